```python
import math
import jax, jax.numpy as jnp
from jax import lax
import numpy as np

D_MODEL = 1024
BATCH = 16
SEQ = 2048
DEPTH = 1

N_SUBLAYERS = 3
N_MOD = 3
D_FF = 2816
POOL_WINDOWS = (2, 4, 8, 16)
POOL_GROUPS = len(POOL_WINDOWS)
POOL_WIDTH = D_MODEL // 2
POOL_GROUP_DIM = POOL_WIDTH // POOL_GROUPS
N_HEADS = 8
QK_NOPE_DIM = 64
QK_ROPE_DIM = 32
V_HEAD_DIM = 64
QK_HEAD_DIM = QK_NOPE_DIM + QK_ROPE_DIM
Q_LORA_RANK = 384
KV_LORA_RANK = 256
MLA_WIDTH = N_HEADS * V_HEAD_DIM
ROPE_THETA = 10000.0
Q_BLOCK = 128
ATTN_SCALE = 1.0 / math.sqrt(QK_HEAD_DIM)
NORM_EPS = 1e-6
IN_SPLITS = (POOL_WIDTH, Q_LORA_RANK, KV_LORA_RANK, QK_ROPE_DIM, D_MODEL, D_MODEL)
IN_WIDTH = sum(IN_SPLITS)

kernel_name = "hybrid_pool_mla_macaron_adaln"


def rms_norm(x, g):
    xf = x.astype(jnp.float32)
    y = xf * lax.rsqrt(jnp.mean(xf * xf, axis=-1, keepdims=True) + NORM_EPS)
    return (y * g.astype(jnp.float32)).astype(x.dtype)


def modulate(h, shift, scale):
    return h * (1.0 + scale[:, None, :]) + shift[:, None, :]


def swiglu(h, w_in, w_out):
    gu = h @ w_in
    g, u = jnp.split(gu, 2, axis=-1)
    return (jax.nn.silu(g) * u) @ w_out


def rope_tables(positions):
    inv_freq = ROPE_THETA ** (-jnp.arange(0, QK_ROPE_DIM, 2, dtype=jnp.float32) / QK_ROPE_DIM)
    ang = positions.astype(jnp.float32)[..., None] * inv_freq
    ang = jnp.concatenate([ang, ang], axis=-1)
    return jnp.cos(ang), jnp.sin(ang)


def apply_rope(x, cos, sin):
    xf = x.astype(jnp.float32)
    x1, x2 = jnp.split(xf, 2, axis=-1)
    rot = jnp.concatenate([-x2, x1], axis=-1)
    return (xf * cos + rot * sin).astype(x.dtype)


def causal_multiscale_pool(u):
    B, S, _ = u.shape
    uf = u.astype(jnp.float32).reshape(B, S, POOL_GROUPS, POOL_GROUP_DIM)
    cs = jnp.pad(jnp.cumsum(uf, axis=1), ((0, 0), (1, 0), (0, 0), (0, 0)))
    t = jnp.arange(S)
    outs = []
    for g, w in enumerate(POOL_WINDOWS):
        hi = cs[:, 1:, g]
        lo = cs[:, jnp.maximum(t + 1 - w, 0), g]
        cnt = jnp.minimum(t + 1, w).astype(jnp.float32)
        outs.append((hi - lo) / cnt[None, :, None])
    pooled = jnp.stack(outs, axis=2)
    return (pooled - uf).astype(u.dtype)


def causal_mla_attention(q_nope, q_rope, k_nope, k_rope, v):
    S = q_nope.shape[1]
    outs = []
    for i in range(S // Q_BLOCK):
        q0, q1 = i * Q_BLOCK, (i + 1) * Q_BLOCK
        s = (jnp.einsum('bqhd,bkhd->bhqk', q_nope[:, q0:q1], k_nope[:, :q1])
             + jnp.einsum('bqhr,bkr->bhqk', q_rope[:, q0:q1], k_rope[:, :q1]))
        s = s.astype(jnp.float32) * ATTN_SCALE
        mask = jnp.arange(q1)[None, :] <= (q0 + jnp.arange(Q_BLOCK))[:, None]
        s = jnp.where(mask, s, jnp.float32(-1e30))
        p = jax.nn.softmax(s, axis=-1).astype(v.dtype)
        outs.append(jnp.einsum('bhqk,bkhd->bqhd', p, v[:, :q1]))
    return jnp.concatenate(outs, axis=1)


def _fwd_setup_inputs(seed: int = 0) -> dict:
    key = jax.random.key(seed)
    ks = jax.random.split(key, 32)
    f32 = jnp.float32

    def w(k, shape, fan_in, gain=1.0):
        return jax.random.normal(k, shape, f32) * (gain * fan_in ** -0.5)

    def gain(k, shape):
        return jnp.ones(shape, f32) + 0.05 * jax.random.normal(k, shape, f32)

    L = DEPTH
    x = jax.random.normal(ks[0], (BATCH, SEQ, D_MODEL), f32)
    c = jax.random.normal(ks[1], (BATCH, D_MODEL), f32)
    offs = jax.random.randint(ks[2], (BATCH, 1), 0, 256, dtype=jnp.int32)
    positions = jnp.arange(SEQ, dtype=jnp.int32)[None, :] + offs
    return {
        "x": x,
        "c": c,
        "positions": positions,
        "w_ada": w(ks[3], (L, D_MODEL, N_SUBLAYERS * N_MOD * D_MODEL), D_MODEL, 0.5),
        "b_ada": 0.05 * jax.random.normal(ks[4], (L, N_SUBLAYERS * N_MOD * D_MODEL), f32),
        "norm_ffn1": gain(ks[5], (L, D_MODEL)),
        "w_ffn1_in": w(ks[6], (L, D_MODEL, 2 * D_FF), D_MODEL),
        "w_ffn1_out": w(ks[7], (L, D_FF, D_MODEL), D_FF),
        "norm_mix": gain(ks[8], (L, D_MODEL)),
        "w_in": w(ks[9], (L, D_MODEL, IN_WIDTH), D_MODEL),
        "pool_grp": w(ks[10], (L, POOL_GROUPS, POOL_GROUP_DIM, POOL_GROUP_DIM), POOL_GROUP_DIM),
        "pool_scale": gain(ks[11], (L, POOL_WIDTH)),
        "w_pool_proj": w(ks[12], (L, POOL_WIDTH, D_MODEL), POOL_WIDTH),
        "q_a_norm": gain(ks[13], (L, Q_LORA_RANK)),
        "w_q_up": w(ks[14], (L, Q_LORA_RANK, N_HEADS * QK_HEAD_DIM), Q_LORA_RANK),
        "kv_a_norm": gain(ks[15], (L, KV_LORA_RANK)),
        "w_kv_up": w(ks[16], (L, KV_LORA_RANK, N_HEADS * (QK_NOPE_DIM + V_HEAD_DIM)), KV_LORA_RANK),
        "q_norm_nope": gain(ks[17], (L, QK_NOPE_DIM)),
        "q_norm_rope": gain(ks[18], (L, QK_ROPE_DIM)),
        "k_norm_nope": gain(ks[19], (L, QK_NOPE_DIM)),
        "k_norm_rope": gain(ks[20], (L, QK_ROPE_DIM)),
        "w_mla_proj": w(ks[21], (L, MLA_WIDTH, D_MODEL), MLA_WIDTH),
        "w_out": w(ks[22], (L, D_MODEL, D_MODEL), D_MODEL),
        "norm_ffn2": gain(ks[23], (L, D_MODEL)),
        "w_ffn2_in": w(ks[24], (L, D_MODEL, 2 * D_FF), D_MODEL),
        "w_ffn2_out": w(ks[25], (L, D_FF, D_MODEL), D_FF),
    }


def _fwd_reference(x, c, positions, w_ada, b_ada, norm_ffn1, w_ffn1_in, w_ffn1_out,
              norm_mix, w_in, pool_grp, pool_scale, w_pool_proj, q_a_norm, w_q_up,
              kv_a_norm, w_kv_up, q_norm_nope, q_norm_rope, k_norm_nope, k_norm_rope,
              w_mla_proj, w_out, norm_ffn2, w_ffn2_in, w_ffn2_out):
    B, S, D = x.shape
    cos, sin = rope_tables(positions)
    c_act = jax.nn.silu(c)
    split_pts = list(np.cumsum(IN_SPLITS)[:-1])

    for l in range(DEPTH):
        mod = (c_act @ w_ada[l] + b_ada[l]).reshape(B, N_SUBLAYERS, N_MOD, D)

        h = modulate(rms_norm(x, norm_ffn1[l]), mod[:, 0, 0], mod[:, 0, 1])
        x = x + 0.5 * mod[:, 0, 2][:, None, :] * swiglu(h, w_ffn1_in[l], w_ffn1_out[l])

        h = modulate(rms_norm(x, norm_mix[l]), mod[:, 1, 0], mod[:, 1, 1])
        u_pool, q_lat, kv_lat, k_rope, g_pool, g_mla = jnp.split(h @ w_in[l], split_pts, axis=-1)

        pooled = causal_multiscale_pool(u_pool).reshape(B, S, POOL_GROUPS, POOL_GROUP_DIM)
        pooled = jnp.einsum('bsgc,gcd->bsgd', pooled, pool_grp[l]).reshape(B, S, POOL_WIDTH)
        br_pool = (pooled * pool_scale[l]) @ w_pool_proj[l]

        q = (rms_norm(q_lat, q_a_norm[l]) @ w_q_up[l]).reshape(B, S, N_HEADS, QK_HEAD_DIM)
        q_nope, q_rope = q[..., :QK_NOPE_DIM], q[..., QK_NOPE_DIM:]
        kv = (rms_norm(kv_lat, kv_a_norm[l]) @ w_kv_up[l]).reshape(
            B, S, N_HEADS, QK_NOPE_DIM + V_HEAD_DIM)
        k_nope, v = kv[..., :QK_NOPE_DIM], kv[..., QK_NOPE_DIM:]
        q_nope = rms_norm(q_nope, q_norm_nope[l])
        k_nope = rms_norm(k_nope, k_norm_nope[l])
        q_rope = apply_rope(rms_norm(q_rope, q_norm_rope[l]), cos[:, :, None, :], sin[:, :, None, :])
        k_rope = apply_rope(rms_norm(k_rope, k_norm_rope[l]), cos, sin)
        attn = causal_mla_attention(q_nope, q_rope, k_nope, k_rope, v).reshape(B, S, MLA_WIDTH)
        br_mla = attn @ w_mla_proj[l]

        merged = jax.nn.sigmoid(g_pool) * br_pool + jax.nn.sigmoid(g_mla) * br_mla
        x = x + mod[:, 1, 2][:, None, :] * (merged @ w_out[l])

        h = modulate(rms_norm(x, norm_ffn2[l]), mod[:, 2, 0], mod[:, 2, 1])
        x = x + 0.5 * mod[:, 2, 2][:, None, :] * swiglu(h, w_ffn2_in[l], w_ffn2_out[l])

    return x


import jax as _jax
import jax.numpy as _jnp

TWIN_FORMAT = 'train_step'
FWD_PARAMS = ['x', 'c', 'positions', 'w_ada', 'b_ada', 'norm_ffn1', 'w_ffn1_in', 'w_ffn1_out', 'norm_mix', 'w_in', 'pool_grp', 'pool_scale', 'w_pool_proj', 'q_a_norm', 'w_q_up', 'kv_a_norm', 'w_kv_up', 'q_norm_nope', 'q_norm_rope', 'k_norm_nope', 'k_norm_rope', 'w_mla_proj', 'w_out', 'norm_ffn2', 'w_ffn2_in', 'w_ffn2_out']
TWIN_WEIGHTS = ['w_ada', 'b_ada', 'norm_ffn1', 'w_ffn1_in', 'w_ffn1_out', 'norm_mix', 'w_in', 'pool_grp', 'pool_scale', 'w_pool_proj', 'q_a_norm', 'w_q_up', 'kv_a_norm', 'w_kv_up', 'q_norm_nope', 'q_norm_rope', 'k_norm_nope', 'k_norm_rope', 'w_mla_proj', 'w_out', 'norm_ffn2', 'w_ffn2_in', 'w_ffn2_out']
TWIN_DIFF_INPUT = 'x'
TWIN_INPUTS = ['x', 'c', 'positions', 'w_ada', 'b_ada', 'norm_ffn1', 'w_ffn1_in', 'w_ffn1_out', 'norm_mix', 'w_in', 'pool_grp', 'pool_scale', 'w_pool_proj', 'q_a_norm', 'w_q_up', 'kv_a_norm', 'w_kv_up', 'q_norm_nope', 'q_norm_rope', 'k_norm_nope', 'k_norm_rope', 'w_mla_proj', 'w_out', 'norm_ffn2', 'w_ffn2_in', 'w_ffn2_out', 'loss_target', 'm_w_ada', 'm_b_ada', 'm_norm_ffn1', 'm_w_ffn1_in', 'm_w_ffn1_out', 'm_norm_mix', 'm_w_in', 'm_pool_grp', 'm_pool_scale', 'm_w_pool_proj', 'm_q_a_norm', 'm_w_q_up', 'm_kv_a_norm', 'm_w_kv_up', 'm_q_norm_nope', 'm_q_norm_rope', 'm_k_norm_nope', 'm_k_norm_rope', 'm_w_mla_proj', 'm_w_out', 'm_norm_ffn2', 'm_w_ffn2_in', 'm_w_ffn2_out', 'v_w_ada', 'v_b_ada', 'v_norm_ffn1', 'v_w_ffn1_in', 'v_w_ffn1_out', 'v_norm_mix', 'v_w_in', 'v_pool_grp', 'v_pool_scale', 'v_w_pool_proj', 'v_q_a_norm', 'v_w_q_up', 'v_kv_a_norm', 'v_w_kv_up', 'v_q_norm_nope', 'v_q_norm_rope', 'v_k_norm_nope', 'v_k_norm_rope', 'v_w_mla_proj', 'v_w_out', 'v_norm_ffn2', 'v_w_ffn2_in', 'v_w_ffn2_out']
TWIN_OUTPUTS = ['loss', 'grad_x', 'grad_w_ada', 'grad_b_ada', 'grad_norm_ffn1', 'grad_w_ffn1_in', 'grad_w_ffn1_out', 'grad_norm_mix', 'grad_w_in', 'grad_pool_grp', 'grad_pool_scale', 'grad_w_pool_proj', 'grad_q_a_norm', 'grad_w_q_up', 'grad_kv_a_norm', 'grad_w_kv_up', 'grad_q_norm_nope', 'grad_q_norm_rope', 'grad_k_norm_nope', 'grad_k_norm_rope', 'grad_w_mla_proj', 'grad_w_out', 'grad_norm_ffn2', 'grad_w_ffn2_in', 'grad_w_ffn2_out', 'delta_w_ada', 'delta_b_ada', 'delta_norm_ffn1', 'delta_w_ffn1_in', 'delta_w_ffn1_out', 'delta_norm_mix', 'delta_w_in', 'delta_pool_grp', 'delta_pool_scale', 'delta_w_pool_proj', 'delta_q_a_norm', 'delta_w_q_up', 'delta_kv_a_norm', 'delta_w_kv_up', 'delta_q_norm_nope', 'delta_q_norm_rope', 'delta_k_norm_nope', 'delta_k_norm_rope', 'delta_w_mla_proj', 'delta_w_out', 'delta_norm_ffn2', 'delta_w_ffn2_in', 'delta_w_ffn2_out', 'new_m_w_ada', 'new_m_b_ada', 'new_m_norm_ffn1', 'new_m_w_ffn1_in', 'new_m_w_ffn1_out', 'new_m_norm_mix', 'new_m_w_in', 'new_m_pool_grp', 'new_m_pool_scale', 'new_m_w_pool_proj', 'new_m_q_a_norm', 'new_m_w_q_up', 'new_m_kv_a_norm', 'new_m_w_kv_up', 'new_m_q_norm_nope', 'new_m_q_norm_rope', 'new_m_k_norm_nope', 'new_m_k_norm_rope', 'new_m_w_mla_proj', 'new_m_w_out', 'new_m_norm_ffn2', 'new_m_w_ffn2_in', 'new_m_w_ffn2_out', 'new_v_w_ada', 'new_v_b_ada', 'new_v_norm_ffn1', 'new_v_w_ffn1_in', 'new_v_w_ffn1_out', 'new_v_norm_mix', 'new_v_w_in', 'new_v_pool_grp', 'new_v_pool_scale', 'new_v_w_pool_proj', 'new_v_q_a_norm', 'new_v_w_q_up', 'new_v_kv_a_norm', 'new_v_w_kv_up', 'new_v_q_norm_nope', 'new_v_q_norm_rope', 'new_v_k_norm_nope', 'new_v_k_norm_rope', 'new_v_w_mla_proj', 'new_v_w_out', 'new_v_norm_ffn2', 'new_v_w_ffn2_in', 'new_v_w_ffn2_out']
TWIN_LEAF_KINDS = {'loss': 'loss', 'grad_x': 'grad_x', 'grad_w_ada': 'grad_w', 'grad_b_ada': 'grad_w', 'grad_norm_ffn1': 'grad_w', 'grad_w_ffn1_in': 'grad_w', 'grad_w_ffn1_out': 'grad_w', 'grad_norm_mix': 'grad_w', 'grad_w_in': 'grad_w', 'grad_pool_grp': 'grad_w', 'grad_pool_scale': 'grad_w', 'grad_w_pool_proj': 'grad_w', 'grad_q_a_norm': 'grad_w', 'grad_w_q_up': 'grad_w', 'grad_kv_a_norm': 'grad_w', 'grad_w_kv_up': 'grad_w', 'grad_q_norm_nope': 'grad_w', 'grad_q_norm_rope': 'grad_w', 'grad_k_norm_nope': 'grad_w', 'grad_k_norm_rope': 'grad_w', 'grad_w_mla_proj': 'grad_w', 'grad_w_out': 'grad_w', 'grad_norm_ffn2': 'grad_w', 'grad_w_ffn2_in': 'grad_w', 'grad_w_ffn2_out': 'grad_w', 'delta_w_ada': 'delta_w', 'delta_b_ada': 'delta_w', 'delta_norm_ffn1': 'delta_w', 'delta_w_ffn1_in': 'delta_w', 'delta_w_ffn1_out': 'delta_w', 'delta_norm_mix': 'delta_w', 'delta_w_in': 'delta_w', 'delta_pool_grp': 'delta_w', 'delta_pool_scale': 'delta_w', 'delta_w_pool_proj': 'delta_w', 'delta_q_a_norm': 'delta_w', 'delta_w_q_up': 'delta_w', 'delta_kv_a_norm': 'delta_w', 'delta_w_kv_up': 'delta_w', 'delta_q_norm_nope': 'delta_w', 'delta_q_norm_rope': 'delta_w', 'delta_k_norm_nope': 'delta_w', 'delta_k_norm_rope': 'delta_w', 'delta_w_mla_proj': 'delta_w', 'delta_w_out': 'delta_w', 'delta_norm_ffn2': 'delta_w', 'delta_w_ffn2_in': 'delta_w', 'delta_w_ffn2_out': 'delta_w', 'new_m_w_ada': 'new_m', 'new_m_b_ada': 'new_m', 'new_m_norm_ffn1': 'new_m', 'new_m_w_ffn1_in': 'new_m', 'new_m_w_ffn1_out': 'new_m', 'new_m_norm_mix': 'new_m', 'new_m_w_in': 'new_m', 'new_m_pool_grp': 'new_m', 'new_m_pool_scale': 'new_m', 'new_m_w_pool_proj': 'new_m', 'new_m_q_a_norm': 'new_m', 'new_m_w_q_up': 'new_m', 'new_m_kv_a_norm': 'new_m', 'new_m_w_kv_up': 'new_m', 'new_m_q_norm_nope': 'new_m', 'new_m_q_norm_rope': 'new_m', 'new_m_k_norm_nope': 'new_m', 'new_m_k_norm_rope': 'new_m', 'new_m_w_mla_proj': 'new_m', 'new_m_w_out': 'new_m', 'new_m_norm_ffn2': 'new_m', 'new_m_w_ffn2_in': 'new_m', 'new_m_w_ffn2_out': 'new_m', 'new_v_w_ada': 'new_v', 'new_v_b_ada': 'new_v', 'new_v_norm_ffn1': 'new_v', 'new_v_w_ffn1_in': 'new_v', 'new_v_w_ffn1_out': 'new_v', 'new_v_norm_mix': 'new_v', 'new_v_w_in': 'new_v', 'new_v_pool_grp': 'new_v', 'new_v_pool_scale': 'new_v', 'new_v_w_pool_proj': 'new_v', 'new_v_q_a_norm': 'new_v', 'new_v_w_q_up': 'new_v', 'new_v_kv_a_norm': 'new_v', 'new_v_w_kv_up': 'new_v', 'new_v_q_norm_nope': 'new_v', 'new_v_q_norm_rope': 'new_v', 'new_v_k_norm_nope': 'new_v', 'new_v_k_norm_rope': 'new_v', 'new_v_w_mla_proj': 'new_v', 'new_v_w_out': 'new_v', 'new_v_norm_ffn2': 'new_v', 'new_v_w_ffn2_in': 'new_v', 'new_v_w_ffn2_out': 'new_v'}


def _forward(args):
    return _fwd_reference(*[args[k] for k in FWD_PARAMS])


def _output_shape():
    out = _jax.eval_shape(lambda: _forward(_fwd_setup_inputs(0)))
    return out.shape, out.dtype

N_MICROBATCH = 1
ADAM_LR = 0.001
ADAM_B1 = 0.9
ADAM_B2 = 0.999
ADAM_EPS = 1e-08
ADAM_WD = 0.01
ADAM_STEP = 10
PER_EXAMPLE_BATCH_AXIS = {'x': 0, 'c': 0, 'positions': 0, 'loss_target': 0}
SHARED_INPUTS = []
_WEIGHT_DTYPES = {'w_ada': _jnp.float32, 'b_ada': _jnp.float32, 'norm_ffn1': _jnp.float32, 'w_ffn1_in': _jnp.float32, 'w_ffn1_out': _jnp.float32, 'norm_mix': _jnp.float32, 'w_in': _jnp.float32, 'pool_grp': _jnp.float32, 'pool_scale': _jnp.float32, 'w_pool_proj': _jnp.float32, 'q_a_norm': _jnp.float32, 'w_q_up': _jnp.float32, 'kv_a_norm': _jnp.float32, 'w_kv_up': _jnp.float32, 'q_norm_nope': _jnp.float32, 'q_norm_rope': _jnp.float32, 'k_norm_nope': _jnp.float32, 'k_norm_rope': _jnp.float32, 'w_mla_proj': _jnp.float32, 'w_out': _jnp.float32, 'norm_ffn2': _jnp.float32, 'w_ffn2_in': _jnp.float32, 'w_ffn2_out': _jnp.float32}
MOMENT_SCALE = {'w_ada': 3.705721e-01, 'b_ada': 8.089251e-01, 'norm_ffn1': 8.518198e-01, 'w_ffn1_in': 2.263210e-02, 'w_ffn1_out': 3.524631e-02, 'norm_mix': 8.785708e-01, 'w_in': 6.640274e-02, 'pool_grp': 1.552796e-01, 'pool_scale': 1.660091e+00, 'w_pool_proj': 8.404325e-02, 'q_a_norm': 1.288839e-02, 'w_q_up': 9.424331e-03, 'kv_a_norm': 3.273086e-01, 'w_kv_up': 6.625441e-02, 'q_norm_nope': 5.295966e-02, 'q_norm_rope': 5.195546e-02, 'k_norm_nope': 5.267510e-02, 'k_norm_rope': 5.337163e-02, 'w_mla_proj': 5.051801e-02, 'w_out': 8.672694e-02, 'norm_ffn2': 8.552326e-01, 'w_ffn2_in': 2.241556e-02, 'w_ffn2_out': 3.495679e-02}


def _to_microbatches(a, axis):
    t = _jnp.moveaxis(a, axis, 0)
    t = t.reshape((N_MICROBATCH, t.shape[0] // N_MICROBATCH) + t.shape[1:])
    return _jnp.moveaxis(t, 1, axis + 1)


def setup_inputs(seed: int = 0) -> dict:
    inp = _fwd_setup_inputs(seed)
    key = _jax.random.fold_in(_jax.random.key(seed), 7919)
    shape, _ = _output_shape()
    out = dict(inp)
    out["loss_target"] = _jax.random.normal(_jax.random.fold_in(key, 0), shape, _jnp.float32)
    for i, name in enumerate(TWIN_WEIGHTS):
        w = inp[name].astype(_jnp.float32)
        if MOMENT_SCALE is None:
            s = _jnp.sqrt(_jnp.mean(_jnp.square(w)) + 1e-30)
        else:
            s = MOMENT_SCALE[name]
        km, kv = _jax.random.split(_jax.random.fold_in(key, i + 1))
        out[name] = w
        out["m_" + name] = s * _jax.random.normal(km, w.shape, _jnp.float32)
        out["v_" + name] = (s * s) * _jax.random.uniform(kv, w.shape, _jnp.float32, 0.5, 1.5)
    if N_MICROBATCH > 1:
        for name, axis in PER_EXAMPLE_BATCH_AXIS.items():
            out[name] = _to_microbatches(out[name], axis)
    return {'x': out['x'], 'c': out['c'], 'positions': out['positions'], 'w_ada': out['w_ada'], 'b_ada': out['b_ada'], 'norm_ffn1': out['norm_ffn1'], 'w_ffn1_in': out['w_ffn1_in'], 'w_ffn1_out': out['w_ffn1_out'], 'norm_mix': out['norm_mix'], 'w_in': out['w_in'], 'pool_grp': out['pool_grp'], 'pool_scale': out['pool_scale'], 'w_pool_proj': out['w_pool_proj'], 'q_a_norm': out['q_a_norm'], 'w_q_up': out['w_q_up'], 'kv_a_norm': out['kv_a_norm'], 'w_kv_up': out['w_kv_up'], 'q_norm_nope': out['q_norm_nope'], 'q_norm_rope': out['q_norm_rope'], 'k_norm_nope': out['k_norm_nope'], 'k_norm_rope': out['k_norm_rope'], 'w_mla_proj': out['w_mla_proj'], 'w_out': out['w_out'], 'norm_ffn2': out['norm_ffn2'], 'w_ffn2_in': out['w_ffn2_in'], 'w_ffn2_out': out['w_ffn2_out'], 'loss_target': out['loss_target'], 'm_w_ada': out['m_w_ada'], 'm_b_ada': out['m_b_ada'], 'm_norm_ffn1': out['m_norm_ffn1'], 'm_w_ffn1_in': out['m_w_ffn1_in'], 'm_w_ffn1_out': out['m_w_ffn1_out'], 'm_norm_mix': out['m_norm_mix'], 'm_w_in': out['m_w_in'], 'm_pool_grp': out['m_pool_grp'], 'm_pool_scale': out['m_pool_scale'], 'm_w_pool_proj': out['m_w_pool_proj'], 'm_q_a_norm': out['m_q_a_norm'], 'm_w_q_up': out['m_w_q_up'], 'm_kv_a_norm': out['m_kv_a_norm'], 'm_w_kv_up': out['m_w_kv_up'], 'm_q_norm_nope': out['m_q_norm_nope'], 'm_q_norm_rope': out['m_q_norm_rope'], 'm_k_norm_nope': out['m_k_norm_nope'], 'm_k_norm_rope': out['m_k_norm_rope'], 'm_w_mla_proj': out['m_w_mla_proj'], 'm_w_out': out['m_w_out'], 'm_norm_ffn2': out['m_norm_ffn2'], 'm_w_ffn2_in': out['m_w_ffn2_in'], 'm_w_ffn2_out': out['m_w_ffn2_out'], 'v_w_ada': out['v_w_ada'], 'v_b_ada': out['v_b_ada'], 'v_norm_ffn1': out['v_norm_ffn1'], 'v_w_ffn1_in': out['v_w_ffn1_in'], 'v_w_ffn1_out': out['v_w_ffn1_out'], 'v_norm_mix': out['v_norm_mix'], 'v_w_in': out['v_w_in'], 'v_pool_grp': out['v_pool_grp'], 'v_pool_scale': out['v_pool_scale'], 'v_w_pool_proj': out['v_w_pool_proj'], 'v_q_a_norm': out['v_q_a_norm'], 'v_w_q_up': out['v_w_q_up'], 'v_kv_a_norm': out['v_kv_a_norm'], 'v_w_kv_up': out['v_w_kv_up'], 'v_q_norm_nope': out['v_q_norm_nope'], 'v_q_norm_rope': out['v_q_norm_rope'], 'v_k_norm_nope': out['v_k_norm_nope'], 'v_k_norm_rope': out['v_k_norm_rope'], 'v_w_mla_proj': out['v_w_mla_proj'], 'v_w_out': out['v_w_out'], 'v_norm_ffn2': out['v_norm_ffn2'], 'v_w_ffn2_in': out['v_w_ffn2_in'], 'v_w_ffn2_out': out['v_w_ffn2_out']}


def _loss(weights, diff, rest, loss_target):
    with _jax.named_scope("forward"):
        args = {**rest, TWIN_DIFF_INPUT: diff, **{k: w.astype(_WEIGHT_DTYPES[k]) for k, w in weights.items()}}
        y = _forward(args)
    with _jax.named_scope("loss_head"):
        err = _jnp.square(y.astype(_jnp.float32) - loss_target)
        return 0.5 * _jnp.sum(_jnp.mean(err, axis=-1)) if err.ndim else 0.5 * err


def _adamw(w, g, m, v):
    m = ADAM_B1 * m + (1.0 - ADAM_B1) * g
    v = ADAM_B2 * v + (1.0 - ADAM_B2) * _jnp.square(g)
    m_hat = m / (1.0 - ADAM_B1 ** ADAM_STEP)
    v_hat = v / (1.0 - ADAM_B2 ** ADAM_STEP)
    delta = -ADAM_LR * (m_hat / (_jnp.sqrt(v_hat) + ADAM_EPS) + ADAM_WD * w)
    return delta, m, v


def reference(x, c, positions, w_ada, b_ada, norm_ffn1, w_ffn1_in, w_ffn1_out, norm_mix, w_in, pool_grp, pool_scale, w_pool_proj, q_a_norm, w_q_up, kv_a_norm, w_kv_up, q_norm_nope, q_norm_rope, k_norm_nope, k_norm_rope, w_mla_proj, w_out, norm_ffn2, w_ffn2_in, w_ffn2_out, loss_target, m_w_ada, m_b_ada, m_norm_ffn1, m_w_ffn1_in, m_w_ffn1_out, m_norm_mix, m_w_in, m_pool_grp, m_pool_scale, m_w_pool_proj, m_q_a_norm, m_w_q_up, m_kv_a_norm, m_w_kv_up, m_q_norm_nope, m_q_norm_rope, m_k_norm_nope, m_k_norm_rope, m_w_mla_proj, m_w_out, m_norm_ffn2, m_w_ffn2_in, m_w_ffn2_out, v_w_ada, v_b_ada, v_norm_ffn1, v_w_ffn1_in, v_w_ffn1_out, v_norm_mix, v_w_in, v_pool_grp, v_pool_scale, v_w_pool_proj, v_q_a_norm, v_w_q_up, v_kv_a_norm, v_w_kv_up, v_q_norm_nope, v_q_norm_rope, v_k_norm_nope, v_k_norm_rope, v_w_mla_proj, v_w_out, v_norm_ffn2, v_w_ffn2_in, v_w_ffn2_out):
    given = dict(x=x, c=c, positions=positions, w_ada=w_ada, b_ada=b_ada, norm_ffn1=norm_ffn1, w_ffn1_in=w_ffn1_in, w_ffn1_out=w_ffn1_out, norm_mix=norm_mix, w_in=w_in, pool_grp=pool_grp, pool_scale=pool_scale, w_pool_proj=w_pool_proj, q_a_norm=q_a_norm, w_q_up=w_q_up, kv_a_norm=kv_a_norm, w_kv_up=w_kv_up, q_norm_nope=q_norm_nope, q_norm_rope=q_norm_rope, k_norm_nope=k_norm_nope, k_norm_rope=k_norm_rope, w_mla_proj=w_mla_proj, w_out=w_out, norm_ffn2=norm_ffn2, w_ffn2_in=w_ffn2_in, w_ffn2_out=w_ffn2_out, loss_target=loss_target, m_w_ada=m_w_ada, m_b_ada=m_b_ada, m_norm_ffn1=m_norm_ffn1, m_w_ffn1_in=m_w_ffn1_in, m_w_ffn1_out=m_w_ffn1_out, m_norm_mix=m_norm_mix, m_w_in=m_w_in, m_pool_grp=m_pool_grp, m_pool_scale=m_pool_scale, m_w_pool_proj=m_w_pool_proj, m_q_a_norm=m_q_a_norm, m_w_q_up=m_w_q_up, m_kv_a_norm=m_kv_a_norm, m_w_kv_up=m_w_kv_up, m_q_norm_nope=m_q_norm_nope, m_q_norm_rope=m_q_norm_rope, m_k_norm_nope=m_k_norm_nope, m_k_norm_rope=m_k_norm_rope, m_w_mla_proj=m_w_mla_proj, m_w_out=m_w_out, m_norm_ffn2=m_norm_ffn2, m_w_ffn2_in=m_w_ffn2_in, m_w_ffn2_out=m_w_ffn2_out, v_w_ada=v_w_ada, v_b_ada=v_b_ada, v_norm_ffn1=v_norm_ffn1, v_w_ffn1_in=v_w_ffn1_in, v_w_ffn1_out=v_w_ffn1_out, v_norm_mix=v_norm_mix, v_w_in=v_w_in, v_pool_grp=v_pool_grp, v_pool_scale=v_pool_scale, v_w_pool_proj=v_w_pool_proj, v_q_a_norm=v_q_a_norm, v_w_q_up=v_w_q_up, v_kv_a_norm=v_kv_a_norm, v_w_kv_up=v_w_kv_up, v_q_norm_nope=v_q_norm_nope, v_q_norm_rope=v_q_norm_rope, v_k_norm_nope=v_k_norm_nope, v_k_norm_rope=v_k_norm_rope, v_w_mla_proj=v_w_mla_proj, v_w_out=v_w_out, v_norm_ffn2=v_norm_ffn2, v_w_ffn2_in=v_w_ffn2_in, v_w_ffn2_out=v_w_ffn2_out)
    weights = {n: given[n] for n in TWIN_WEIGHTS}
    shared = {n: given[n] for n in SHARED_INPUTS}
    per_example = {n: given[n] for n in ['x', 'c', 'positions']}
    grad_fn = _jax.value_and_grad(_loss, argnums=(0, 1))

    def one_microbatch(ex, loss_target):
        ex = dict(ex)
        diff = ex.pop(TWIN_DIFF_INPUT)
        return grad_fn(weights, diff, {**shared, **ex}, loss_target)

    if N_MICROBATCH == 1:
        loss, (grad_w, grad_x) = one_microbatch(per_example, given["loss_target"])
    else:
        def body(carry, xs):
            loss_sum, grad_sum = carry
            l_k, (gw_k, gx_k) = one_microbatch(xs[0], xs[1])
            with _jax.named_scope("update"):
                return (loss_sum + l_k, _jax.tree.map(_jnp.add, grad_sum, gw_k)), gx_k

        init = (_jnp.zeros((), _jnp.float32), _jax.tree.map(_jnp.zeros_like, weights))
        (loss, grad_w), grad_x = _jax.lax.scan(body, init, (per_example, given["loss_target"]))
    with _jax.named_scope("update"):
        delta_w, new_m, new_v = {}, {}, {}
        for n in TWIN_WEIGHTS:
            delta_w[n], new_m[n], new_v[n] = _adamw(weights[n], grad_w[n], given["m_" + n], given["v_" + n])
    return (loss, grad_x, *[grad_w[n] for n in TWIN_WEIGHTS], *[delta_w[n] for n in TWIN_WEIGHTS],
            *[new_m[n] for n in TWIN_WEIGHTS], *[new_v[n] for n in TWIN_WEIGHTS])
```

```python
import functools
import math

import jax
import jax.numpy as jnp
from jax import lax
from jax.experimental import pallas as pl
from jax.experimental.pallas import tpu as pltpu

F32 = jnp.float32
MX = jnp.bfloat16

D = 1024
DFF = 2816
NH = 8
POOL_W = 512
POOL_G = 4
QL = 384
KVL = 256
ROPE = 32
NOPE = 64
LANE = 128
SUBLANE = 8
EPS = 1e-6
ATTN_SCALE = 1.0 / math.sqrt(96.0)
NEG = -1e30

Z_UP, Z_QL, Z_KV, Z_KR, Z_GP, Z_GM, Z_W = 0, 512, 896, 1152, 1280, 2304, 3328
KR_LANE = 64
LAT_W = 1280

ADAM_LR, ADAM_B1, ADAM_B2, ADAM_EPS, ADAM_WD, ADAM_STEP = 0.001, 0.9, 0.999, 1e-08, 0.01, 10

VMEM_LIMIT = 48 * 1024 * 1024
MESH = pl.DeviceIdType.MESH
N_DEV = 8
N_CHIP = 4


def _pcall(body, *, name, out_shape, grid=(), in_specs=None, out_specs=None, scratch=(), grid_spec=None, aliases=None):
    params = pltpu.CompilerParams(vmem_limit_bytes=VMEM_LIMIT)
    kw = dict(name=name, out_shape=out_shape, compiler_params=params)
    if aliases:
        kw["input_output_aliases"] = aliases
    if grid_spec is not None:
        return pl.pallas_call(body, grid_spec=grid_spec, **kw)
    return pl.pallas_call(body, grid=grid, in_specs=in_specs, out_specs=out_specs, scratch_shapes=scratch, **kw)


def _pick(dim, target):
    best = None
    for t in range(LANE, min(dim, target) + 1, LANE):
        if dim % t == 0:
            best = t
    return dim if best is None else best


def _sds(shape, dtype):
    return jax.ShapeDtypeStruct(shape, dtype)


def _mm(a, b, *, mode, out_dtype, name, tm=512, tn=512, tk=1024):
    if mode == "nn":
        (M, K), (K2, N) = a.shape, b.shape
    elif mode == "nt":
        (M, K), (N, K2) = a.shape, b.shape
    else:
        (K, M), (K2, N) = a.shape, b.shape
    assert K == K2, (name, a.shape, b.shape)
    tm, tn, tk = _pick(M, tm), _pick(N, tn), _pick(K, tk)
    nk = K // tk
    if mode == "tn":
        a_spec = pl.BlockSpec((tk, tm), lambda i, j, k: (k, i))
    else:
        a_spec = pl.BlockSpec((tm, tk), lambda i, j, k: (i, k))
    if mode == "nt":
        b_spec = pl.BlockSpec((tn, tk), lambda i, j, k: (j, k))
    else:
        b_spec = pl.BlockSpec((tk, tn), lambda i, j, k: (k, j))
    dn = {"nn": (((1,), (0,)), ((), ())), "nt": (((1,), (1,)), ((), ())), "tn": (((0,), (0,)), ((), ()))}[mode]

    def body(a_ref, b_ref, o_ref, acc_ref):
        k = pl.program_id(2)
        part = lax.dot_general(a_ref[...].astype(MX), b_ref[...].astype(MX), dn, preferred_element_type=F32)
        if nk == 1:
            o_ref[...] = part.astype(o_ref.dtype)
        else:
            @pl.when(k == 0)
            def _():
                acc_ref[...] = part

            @pl.when(k > 0)
            def _():
                acc_ref[...] += part

            @pl.when(k == nk - 1)
            def _():
                o_ref[...] = acc_ref[...].astype(o_ref.dtype)

    return _pcall(
        body, name=name, out_shape=_sds((M, N), out_dtype), grid=(M // tm, N // tn, nk),
        in_specs=[a_spec, b_spec], out_specs=pl.BlockSpec((tm, tn), lambda i, j, k: (i, j)),
        scratch=[pltpu.VMEM((tm, tn), F32)],
    )(a, b)


def _rsq(x):
    return lax.rsqrt(jnp.mean(x * x, axis=-1, keepdims=True) + EPS)


def _sigmoid(x):
    return 1.0 / (1.0 + jnp.exp(-x))


def _row_spec(tm, w):
    return pl.BlockSpec((tm, w), lambda i: (i, 0))


def _fwd_block(x_prev, f_prev, mod3, gain, *, sub, coef, name, tm=256):
    T, dm = x_prev.shape
    tps = (T // mod3.shape[0]) // tm
    has_f = f_prev is not None
    mod_spec = pl.BlockSpec((1, 9, dm), lambda i: (i // tps, 0, 0))
    vec_spec = pl.BlockSpec((1, dm), lambda i: (0, 0))

    def body(*refs):
        if has_f:
            x_ref, f_ref, mod_ref, n_ref, xo_ref, h_ref = refs
            x = x_ref[...] + coef * mod_ref[0, 3 * sub - 1:3 * sub, :] * f_ref[...]
            xo_ref[...] = x
        else:
            x_ref, mod_ref, n_ref, h_ref = refs
            x = x_ref[...]
        xn = x * _rsq(x) * n_ref[...]
        h = xn * (1.0 + mod_ref[0, 3 * sub + 1:3 * sub + 2, :]) + mod_ref[0, 3 * sub:3 * sub + 1, :]
        h_ref[...] = h.astype(h_ref.dtype)

    row = _row_spec(tm, dm)
    if has_f:
        return _pcall(body, name=name, grid=(T // tm,), in_specs=[row, row, mod_spec, vec_spec], out_specs=[row, row],
                      out_shape=[_sds((T, dm), F32), _sds((T, dm), MX)])(x_prev, f_prev, mod3, gain)
    return _pcall(body, name=name, grid=(T // tm,), in_specs=[row, mod_spec, vec_spec], out_specs=row,
                  out_shape=_sds((T, dm), MX))(x_prev, mod3, gain)


def _final(x2, f2, tgt, mod3, *, name, tm=256):
    T, dm = x2.shape
    tps = (T // mod3.shape[0]) // tm
    mod_spec = pl.BlockSpec((1, 9, dm), lambda i: (i // tps, 0, 0))
    row = _row_spec(tm, dm)
    stat_spec = pl.BlockSpec((1, SUBLANE, dm), lambda i: (i // tps, 0, 0))
    loss_spec = pl.BlockSpec((SUBLANE, LANE), lambda i: (0, 0))

    def body(x_ref, f_ref, t_ref, mod_ref, dy_ref, df_ref, st_ref, loss_ref):
        i = pl.program_id(0)
        g = mod_ref[0, 8:9, :]
        f = f_ref[...]
        err = x_ref[...] + 0.5 * g * f - t_ref[...]
        dy = err * (1.0 / dm)
        dy_ref[...] = dy
        df_ref[...] = (0.5 * g * dy).astype(df_ref.dtype)
        dgate = jnp.sum(0.5 * dy * f, axis=0, keepdims=True)
        part = 0.5 * jnp.sum(jnp.sum(err * err, axis=0, keepdims=True), axis=1, keepdims=True) * (1.0 / dm)

        @pl.when(i % tps == 0)
        def _():
            st_ref[...] = jnp.zeros_like(st_ref)

        @pl.when(i == 0)
        def _():
            loss_ref[...] = jnp.zeros_like(loss_ref)

        st_ref[0, 0:1, :] += dgate
        loss_ref[...] += jnp.broadcast_to(part, loss_ref.shape)

    return _pcall(body, name=name, grid=(T // tm,), in_specs=[row, row, row, mod_spec],
                  out_specs=[row, row, stat_spec, loss_spec],
                  out_shape=[_sds((T, dm), F32), _sds((T, dm), MX), _sds((mod3.shape[0], SUBLANE, dm), F32),
                             _sds((SUBLANE, LANE), F32)])(x2, f2, tgt, mod3)


def _bwd_block(x_cur, dh, dx_in, f_prev, mod3, gain, *, sub, coef, name, tm=256):
    T, dm = x_cur.shape
    nb = mod3.shape[0]
    tps = (T // nb) // tm
    has_f = f_prev is not None
    mod_spec = pl.BlockSpec((1, 9, dm), lambda i: (i // tps, 0, 0))
    vec_spec = pl.BlockSpec((1, dm), lambda i: (0, 0))
    stat_spec = pl.BlockSpec((1, SUBLANE, dm), lambda i: (i // tps, 0, 0))
    row = _row_spec(tm, dm)

    def body(*refs):
        if has_f:
            x_ref, dh_ref, dxi_ref, f_ref, mod_ref, n_ref, dx_ref, df_ref, st_ref = refs
        else:
            x_ref, dh_ref, dxi_ref, mod_ref, n_ref, dx_ref, st_ref = refs
        i = pl.program_id(0)
        x = x_ref[...]
        r = _rsq(x)
        xhat = x * r
        n = n_ref[...]
        dhv = dh_ref[...]
        d_shift = jnp.sum(dhv, axis=0, keepdims=True)
        d_scale = jnp.sum(dhv * (xhat * n), axis=0, keepdims=True)
        dxn = dhv * (1.0 + mod_ref[0, 3 * sub + 1:3 * sub + 2, :])
        d_gain = jnp.sum(dxn * xhat, axis=0, keepdims=True)
        dxhat = dxn * n
        dx = dxi_ref[...] + r * (dxhat - xhat * jnp.mean(dxhat * xhat, axis=-1, keepdims=True))
        dx_ref[...] = dx

        @pl.when(i % tps == 0)
        def _():
            st_ref[...] = jnp.zeros_like(st_ref)

        st_ref[0, 0:1, :] += d_shift
        st_ref[0, 1:2, :] += d_scale
        st_ref[0, 2:3, :] += d_gain
        if has_f:
            f = f_ref[...]
            st_ref[0, 3:4, :] += jnp.sum(coef * dx * f, axis=0, keepdims=True)
            df_ref[...] = (coef * mod_ref[0, 3 * sub - 1:3 * sub, :] * dx).astype(df_ref.dtype)

    st_shape = _sds((nb, SUBLANE, dm), F32)
    if has_f:
        return _pcall(body, name=name, grid=(T // tm,), in_specs=[row, row, row, row, mod_spec, vec_spec],
                      out_specs=[row, row, stat_spec],
                      out_shape=[_sds((T, dm), F32), _sds((T, dm), MX), st_shape])(x_cur, dh, dx_in, f_prev, mod3, gain)
    return _pcall(body, name=name, grid=(T // tm,), in_specs=[row, row, row, mod_spec, vec_spec],
                  out_specs=[row, stat_spec], out_shape=[_sds((T, dm), F32), st_shape])(x_cur, dh, dx_in, mod3, gain)


def _swiglu_fwd(gu, *, name, tm=128):
    T, w = gu.shape
    half = w // 2

    def body(gu_ref, a_ref):
        g = gu_ref[:, :half]
        u = gu_ref[:, half:]
        a_ref[...] = (g * _sigmoid(g) * u).astype(a_ref.dtype)

    return _pcall(body, name=name, grid=(T // tm,), in_specs=[_row_spec(tm, w)], out_specs=_row_spec(tm, half),
                  out_shape=_sds((T, half), MX))(gu)


def _swiglu_bwd(gu, da, *, name, tm=128):
    T, w = gu.shape
    half = w // 2

    def body(gu_ref, da_ref, d_ref):
        g = gu_ref[:, :half]
        u = gu_ref[:, half:]
        dav = da_ref[...]
        s = _sigmoid(g)
        d_ref[:, :half] = (dav * u * (s * (1.0 + g * (1.0 - s)))).astype(d_ref.dtype)
        d_ref[:, half:] = (dav * (g * s)).astype(d_ref.dtype)

    return _pcall(body, name=name, grid=(T // tm,), in_specs=[_row_spec(tm, w), _row_spec(tm, half)],
                  out_specs=_row_spec(tm, w), out_shape=_sds((T, w), MX))(gu, da)


def _merge_fwd(z, br_pool, br_mla, *, name, tm=256):
    T = z.shape[0]

    def body(z_ref, bp_ref, bm_ref, o_ref):
        gp = z_ref[:, Z_GP:Z_GP + D]
        gm = z_ref[:, Z_GM:Z_GM + D]
        o_ref[...] = (_sigmoid(gp) * bp_ref[...] + _sigmoid(gm) * bm_ref[...]).astype(o_ref.dtype)

    return _pcall(body, name=name, grid=(T // tm,), in_specs=[_row_spec(tm, Z_W), _row_spec(tm, D), _row_spec(tm, D)],
                  out_specs=_row_spec(tm, D), out_shape=_sds((T, D), MX))(z, br_pool, br_mla)


def _merge_bwd(z, br_pool, br_mla, dmerged, *, name, tm=256):
    T = z.shape[0]

    def body(z_ref, bp_ref, bm_ref, dm_ref, dbp_ref, dbm_ref, dg_ref):
        dm = dm_ref[...]
        sp = _sigmoid(z_ref[:, Z_GP:Z_GP + D])
        sm = _sigmoid(z_ref[:, Z_GM:Z_GM + D])
        dbp_ref[...] = (dm * sp).astype(dbp_ref.dtype)
        dbm_ref[...] = (dm * sm).astype(dbm_ref.dtype)
        dg_ref[:, :D] = (dm * bp_ref[...] * sp * (1.0 - sp)).astype(dg_ref.dtype)
        dg_ref[:, D:] = (dm * bm_ref[...] * sm * (1.0 - sm)).astype(dg_ref.dtype)

    return _pcall(body, name=name, grid=(T // tm,),
                  in_specs=[_row_spec(tm, Z_W), _row_spec(tm, D), _row_spec(tm, D), _row_spec(tm, D)],
                  out_specs=[_row_spec(tm, D), _row_spec(tm, D), _row_spec(tm, 2 * D)],
                  out_shape=[_sds((T, D), MX), _sds((T, D), MX), _sds((T, 2 * D), MX)])(z, br_pool, br_mla, dmerged)


def _shift_down(x, k, row):
    return jnp.where(row >= k, pltpu.roll(x, k, 0), 0.0)


def _shift_up(x, k, row, n):
    return jnp.where(row < n - k, pltpu.roll(x, n - k, 0), 0.0)


def _pool_fwd(z, pool_grp, pool_scale, *, nb, name):
    T = z.shape[0]
    S = T // nb
    blk = pl.BlockSpec((S, LANE), lambda b, g: (b, g))

    def body(u_ref, w_ref, s_ref, pooled_ref, mixed_ref, scaled_ref):
        g = pl.program_id(1)
        u = u_ref[...]
        row = lax.broadcasted_iota(jnp.int32, u.shape, 0)
        s2 = u + _shift_down(u, 1, row)
        s4 = s2 + _shift_down(s2, 2, row)
        s8 = s4 + _shift_down(s4, 4, row)
        s16 = s8 + _shift_down(s8, 8, row)
        win = jnp.where(g == 0, s2, jnp.where(g == 1, s4, jnp.where(g == 2, s8, s16)))
        width = lax.shift_left(jnp.int32(2), g)
        cnt = jnp.minimum(row + 1, width).astype(F32)
        pooled = (win / cnt - u).astype(MX)
        pooled_ref[...] = pooled
        mixed = jnp.dot(pooled, w_ref[0], preferred_element_type=F32)
        mixed_ref[...] = mixed
        scaled_ref[...] = (mixed * s_ref[...]).astype(scaled_ref.dtype)

    return _pcall(body, name=name, grid=(nb, POOL_G),
                  in_specs=[blk, pl.BlockSpec((1, LANE, LANE), lambda b, g: (g, 0, 0)),
                            pl.BlockSpec((1, LANE), lambda b, g: (0, g))],
                  out_specs=[blk, blk, blk],
                  out_shape=[_sds((T, POOL_W), MX), _sds((T, POOL_W), F32), _sds((T, POOL_W), MX)])(z, pool_grp, pool_scale)


def _pool_bwd(dscaled, mixed, pooled, pool_grp, pool_scale, *, nb, name):
    T = dscaled.shape[0]
    S = T // nb
    blk = pl.BlockSpec((S, LANE), lambda b, g: (b, g))

    def body(ds_ref, mixed_ref, pooled_ref, w_ref, s_ref, du_ref, dw_ref, dsc_ref):
        g = pl.program_id(1)
        ds = ds_ref[...]
        dsc_ref[0] = jnp.sum(ds * mixed_ref[...], axis=0, keepdims=True)
        dmixed = (ds * s_ref[...]).astype(MX)
        dw_ref[0, 0] = lax.dot_general(pooled_ref[...], dmixed, (((0,), (0,)), ((), ())), preferred_element_type=F32)
        dpooled = lax.dot_general(dmixed, w_ref[0], (((1,), (1,)), ((), ())), preferred_element_type=F32)
        row = lax.broadcasted_iota(jnp.int32, dpooled.shape, 0)
        width = lax.shift_left(jnp.int32(2), g)
        q = dpooled / jnp.minimum(row + 1, width).astype(F32)
        r2 = q + _shift_up(q, 1, row, S)
        r4 = r2 + _shift_up(r2, 2, row, S)
        r8 = r4 + _shift_up(r4, 4, row, S)
        r16 = r8 + _shift_up(r8, 8, row, S)
        win = jnp.where(g == 0, r2, jnp.where(g == 1, r4, jnp.where(g == 2, r8, r16)))
        du_ref[...] = (win - dpooled).astype(du_ref.dtype)

    return _pcall(body, name=name, grid=(nb, POOL_G),
                  in_specs=[blk, blk, blk, pl.BlockSpec((1, LANE, LANE), lambda b, g: (g, 0, 0)),
                            pl.BlockSpec((1, LANE), lambda b, g: (0, g))],
                  out_specs=[blk, pl.BlockSpec((1, 1, LANE, LANE), lambda b, g: (b, g, 0, 0)),
                             pl.BlockSpec((1, 1, LANE), lambda b, g: (b, 0, g))],
                  out_shape=[_sds((T, POOL_W), MX), _sds((nb, POOL_G, LANE, LANE), F32), _sds((nb, 1, POOL_W), F32)],
                  )(dscaled, mixed, pooled, pool_grp, pool_scale)


def _lat_fwd(z, q_gain, kv_gain, *, name, tm=256):
    T = z.shape[0]

    def body(z_ref, qg_ref, kg_ref, qn_ref, kvn_ref):
        ql = z_ref[:, Z_QL:Z_QL + QL]
        kv = z_ref[:, Z_KV:Z_KV + KVL]
        qn_ref[...] = (ql * _rsq(ql) * qg_ref[...]).astype(qn_ref.dtype)
        kvn_ref[...] = (kv * _rsq(kv) * kg_ref[...]).astype(kvn_ref.dtype)

    return _pcall(body, name=name, grid=(T // tm,),
                  in_specs=[_row_spec(tm, LAT_W), pl.BlockSpec((1, QL), lambda i: (0, 0)), pl.BlockSpec((1, KVL), lambda i: (0, 0))],
                  out_specs=[_row_spec(tm, QL), _row_spec(tm, KVL)],
                  out_shape=[_sds((T, QL), MX), _sds((T, KVL), MX)])(z, q_gain, kv_gain)


def _lat_bwd(z, dqn, dkvn, q_gain, kv_gain, *, nb, name, tm=256):
    T = z.shape[0]
    tps = (T // nb) // tm

    def norm_bwd(x, dy, gain):
        r = _rsq(x)
        xhat = x * r
        dgain = jnp.sum(dy * xhat, axis=0, keepdims=True)
        dxhat = dy * gain
        return r * (dxhat - xhat * jnp.mean(dxhat * xhat, axis=-1, keepdims=True)), dgain

    def body(z_ref, dq_ref, dkv_ref, qg_ref, kg_ref, dql_ref, dkvl_ref, sq_ref, sk_ref):
        i = pl.program_id(0)
        dql, dqg = norm_bwd(z_ref[:, Z_QL:Z_QL + QL], dq_ref[...], qg_ref[...])
        dkvl, dkg = norm_bwd(z_ref[:, Z_KV:Z_KV + KVL], dkv_ref[...], kg_ref[...])
        dql_ref[...] = dql.astype(dql_ref.dtype)
        dkvl_ref[...] = dkvl.astype(dkvl_ref.dtype)

        @pl.when(i % tps == 0)
        def _():
            sq_ref[...] = jnp.zeros_like(sq_ref)
            sk_ref[...] = jnp.zeros_like(sk_ref)

        sq_ref[0, 0:1, :] += dqg
        sk_ref[0, 0:1, :] += dkg

    return _pcall(body, name=name, grid=(T // tm,),
                  in_specs=[_row_spec(tm, LAT_W), _row_spec(tm, QL), _row_spec(tm, KVL),
                            pl.BlockSpec((1, QL), lambda i: (0, 0)), pl.BlockSpec((1, KVL), lambda i: (0, 0))],
                  out_specs=[_row_spec(tm, QL), _row_spec(tm, KVL),
                             pl.BlockSpec((1, SUBLANE, QL), lambda i: (i // tps, 0, 0)),
                             pl.BlockSpec((1, SUBLANE, KVL), lambda i: (i // tps, 0, 0))],
                  out_shape=[_sds((T, QL), MX), _sds((T, KVL), MX), _sds((nb, SUBLANE, QL), F32), _sds((nb, SUBLANE, KVL), F32)],
                  )(z, dqn, dkvn, q_gain, kv_gain)


def _lane_masks(shape):
    lane = lax.broadcasted_iota(jnp.int32, shape, len(shape) - 1)
    m_n = lane < NOPE
    m_r = jnp.logical_and(lane >= KR_LANE, lane < KR_LANE + ROPE)
    first_half = lane < KR_LANE + ROPE // 2
    return m_n, m_r, first_half


def _rot(y, first_half):
    return jnp.where(first_half, -pltpu.roll(y, LANE - ROPE // 2, 1), pltpu.roll(y, ROPE // 2, 1))


def _rot_t(v, first_half, m_r):
    return jnp.where(m_r, jnp.where(first_half, pltpu.roll(v, LANE - ROPE // 2, 1), -pltpu.roll(v, ROPE // 2, 1)), 0.0)


def _prep_fwd(qp, kvp, z, cos, sin, q_gain, k_gain, *, name, tm=256):
    T = qp.shape[0]
    slab = pl.BlockSpec((tm, LANE), lambda i: (i, 0))
    kr_spec = pl.BlockSpec((tm, LANE), lambda i: (i, Z_KR // LANE))
    vec = pl.BlockSpec((1, LANE), lambda i: (0, 0))

    def body(qp_ref, kvp_ref, kr_ref, cos_ref, sin_ref, qg_ref, kg_ref, q_ref, k_ref, v_ref):
        m_n, m_r, first_half = _lane_masks((tm, LANE))
        c = cos_ref[...]
        s = sin_ref[...]
        qg = qg_ref[...]
        kg = kg_ref[...]
        xr = kr_ref[...]
        rr = lax.rsqrt(jnp.sum(xr * xr, axis=-1, keepdims=True) * (1.0 / ROPE) + EPS)
        yr = xr * rr * kg
        kr = jnp.where(m_r, yr * c + _rot(yr, first_half) * s, 0.0)
        for h in range(NH):
            x = qp_ref[:, h * LANE:(h + 1) * LANE]
            x2 = x * x
            rn = lax.rsqrt(jnp.sum(jnp.where(m_n, x2, 0.0), axis=-1, keepdims=True) * (1.0 / NOPE) + EPS)
            rq = lax.rsqrt(jnp.sum(jnp.where(m_r, x2, 0.0), axis=-1, keepdims=True) * (1.0 / ROPE) + EPS)
            y = x * jnp.where(m_n, rn, jnp.where(m_r, rq, 0.0)) * qg
            q_ref[:, h * LANE:(h + 1) * LANE] = (y * c + _rot(y, first_half) * s).astype(q_ref.dtype)
            xk = kvp_ref[:, h * LANE:(h + 1) * LANE]
            rk = lax.rsqrt(jnp.sum(jnp.where(m_n, xk * xk, 0.0), axis=-1, keepdims=True) * (1.0 / NOPE) + EPS)
            k_ref[:, h * LANE:(h + 1) * LANE] = (jnp.where(m_n, xk * rk * kg, 0.0) + kr).astype(k_ref.dtype)
        v_ref[...] = kvp_ref[:, NH * LANE:].astype(v_ref.dtype)

    return _pcall(body, name=name, grid=(T // tm,),
                  in_specs=[_row_spec(tm, NH * LANE), _row_spec(tm, NH * LANE + NH * NOPE), kr_spec, slab, slab, vec, vec],
                  out_specs=[_row_spec(tm, NH * LANE), _row_spec(tm, NH * LANE), _row_spec(tm, NH * NOPE)],
                  out_shape=[_sds((T, NH * LANE), MX), _sds((T, NH * LANE), MX), _sds((T, NH * NOPE), MX)],
                  )(qp, kvp, z, cos, sin, q_gain, k_gain)


def _prep_bwd(dq, dk, dv, qp, kvp, z, cos, sin, q_gain, k_gain, *, nb, name, tm=256):
    T = qp.shape[0]
    tps = (T // nb) // tm
    slab = pl.BlockSpec((tm, LANE), lambda i: (i, 0))
    kr_spec = pl.BlockSpec((tm, LANE), lambda i: (i, Z_KR // LANE))
    vec = pl.BlockSpec((1, LANE), lambda i: (0, 0))

    def body(dq_ref, dk_ref, dv_ref, qp_ref, kvp_ref, kr_ref, cos_ref, sin_ref, qg_ref, kg_ref,
             dqp_ref, dkvp_ref, dkr_ref, st_ref):
        i = pl.program_id(0)
        m_n, m_r, first_half = _lane_masks((tm, LANE))
        c = cos_ref[...]
        s = sin_ref[...]
        qg = qg_ref[...]
        kg = kg_ref[...]
        dqg = jnp.zeros((1, LANE), F32)
        dkg = jnp.zeros((1, LANE), F32)
        dkr_sum = jnp.zeros((tm, LANE), F32)
        for h in range(NH):
            x = qp_ref[:, h * LANE:(h + 1) * LANE]
            x2 = x * x
            rn = lax.rsqrt(jnp.sum(jnp.where(m_n, x2, 0.0), axis=-1, keepdims=True) * (1.0 / NOPE) + EPS)
            rq = lax.rsqrt(jnp.sum(jnp.where(m_r, x2, 0.0), axis=-1, keepdims=True) * (1.0 / ROPE) + EPS)
            rfac = jnp.where(m_n, rn, jnp.where(m_r, rq, 0.0))
            xhat = x * rfac
            do = dq_ref[:, h * LANE:(h + 1) * LANE]
            dy = do * c + _rot_t(do * s, first_half, m_r)
            dqg = dqg + jnp.sum(dy * xhat, axis=0, keepdims=True)
            dxhat = dy * qg
            t = dxhat * xhat
            mean_n = jnp.sum(jnp.where(m_n, t, 0.0), axis=-1, keepdims=True) * (1.0 / NOPE)
            mean_r = jnp.sum(jnp.where(m_r, t, 0.0), axis=-1, keepdims=True) * (1.0 / ROPE)
            dqp_ref[:, h * LANE:(h + 1) * LANE] = (
                rfac * (dxhat - xhat * jnp.where(m_n, mean_n, jnp.where(m_r, mean_r, 0.0)))).astype(dqp_ref.dtype)

            xk = kvp_ref[:, h * LANE:(h + 1) * LANE]
            rk = lax.rsqrt(jnp.sum(jnp.where(m_n, xk * xk, 0.0), axis=-1, keepdims=True) * (1.0 / NOPE) + EPS)
            khat = jnp.where(m_n, xk * rk, 0.0)
            dko = dk_ref[:, h * LANE:(h + 1) * LANE]
            dkn = jnp.where(m_n, dko, 0.0)
            dkg = dkg + jnp.sum(dkn * khat, axis=0, keepdims=True)
            dkhat = dkn * kg
            mean_k = jnp.sum(dkhat * khat, axis=-1, keepdims=True) * (1.0 / NOPE)
            dkvp_ref[:, h * LANE:(h + 1) * LANE] = jnp.where(m_n, rk * (dkhat - khat * mean_k), 0.0).astype(dkvp_ref.dtype)
            dkr_sum = dkr_sum + jnp.where(m_r, dko, 0.0)
        dkvp_ref[:, NH * LANE:] = dv_ref[...].astype(dkvp_ref.dtype)

        xr = kr_ref[...]
        rr = lax.rsqrt(jnp.sum(xr * xr, axis=-1, keepdims=True) * (1.0 / ROPE) + EPS)
        rhat = xr * rr
        dyr = dkr_sum * c + _rot_t(dkr_sum * s, first_half, m_r)
        dkg = dkg + jnp.sum(dyr * rhat, axis=0, keepdims=True)
        drhat = dyr * kg
        mean_kr = jnp.sum(drhat * rhat, axis=-1, keepdims=True) * (1.0 / ROPE)
        dkr_ref[...] = jnp.where(m_r, rr * (drhat - rhat * mean_kr), 0.0).astype(dkr_ref.dtype)

        @pl.when(i % tps == 0)
        def _():
            st_ref[...] = jnp.zeros_like(st_ref)

        st_ref[0, 0:1, :] += dqg
        st_ref[0, 1:2, :] += dkg

    return _pcall(body, name=name, grid=(T // tm,),
                  in_specs=[_row_spec(tm, NH * LANE), _row_spec(tm, NH * LANE), _row_spec(tm, NH * NOPE),
                            _row_spec(tm, NH * LANE), _row_spec(tm, NH * LANE + NH * NOPE), kr_spec, slab, slab, vec, vec],
                  out_specs=[_row_spec(tm, NH * LANE), _row_spec(tm, NH * LANE + NH * NOPE), slab,
                             pl.BlockSpec((1, SUBLANE, LANE), lambda i: (i // tps, 0, 0))],
                  out_shape=[_sds((T, NH * LANE), MX), _sds((T, NH * LANE + NH * NOPE), MX), _sds((T, LANE), MX),
                             _sds((nb, SUBLANE, LANE), F32)],
                  )(dq, dk, dv, qp, kvp, z, cos, sin, q_gain, k_gain)


def _causal_mask(qi, kj, tq, tk):
    r = lax.broadcasted_iota(jnp.int32, (tq, tk), 0) + qi * tq
    c = lax.broadcasted_iota(jnp.int32, (tq, tk), 1) + kj * tk
    return c <= r


def _attn_fwd(q, k, v, *, nb, name, tq=256):
    T = q.shape[0]
    S = T // nb
    nq = S // tq
    tk = tq
    npair = NH // 2

    def body(q_ref, k_ref, v_ref, o_ref, lse_ref):
        qi = pl.program_id(2)
        lane = lax.broadcasted_iota(jnp.int32, (tq, LANE), 1)
        outs, lses = [], []
        for hh in range(2):
            qh = q_ref[:, hh * LANE:(hh + 1) * LANE]

            def step(j, carry, hh=hh, qh=qh):
                m, l, acc = carry
                kb = k_ref[pl.ds(pl.multiple_of(j * tk, tk), tk), hh * LANE:(hh + 1) * LANE]
                vb = v_ref[pl.ds(pl.multiple_of(j * tk, tk), tk), :]
                s = lax.dot_general(qh, kb, (((1,), (1,)), ((), ())), preferred_element_type=F32) * ATTN_SCALE
                s = jnp.where(_causal_mask(qi, j, tq, tk), s, NEG)
                m_new = jnp.maximum(m, jnp.max(s, axis=-1, keepdims=True))
                p = jnp.exp(s - m_new)
                alpha = jnp.exp(m - m_new)
                l = alpha * l + jnp.sum(p, axis=-1, keepdims=True)
                acc = alpha * acc + jnp.dot(p.astype(MX), vb, preferred_element_type=F32)
                return m_new, l, acc

            m, l, acc = lax.fori_loop(0, qi + 1, step, (jnp.full((tq, 1), NEG, F32), jnp.zeros((tq, 1), F32),
                                                        jnp.zeros((tq, LANE), F32)))
            outs.append(acc / l)
            lses.append(m + jnp.log(l))
        o_ref[...] = jnp.where(lane < NOPE, outs[0], outs[1]).astype(o_ref.dtype)
        lse_ref[...] = jnp.where(lane < NOPE, lses[0], lses[1])

    return _pcall(body, name=name, grid=(nb, npair, nq),
                  in_specs=[pl.BlockSpec((tq, 2 * LANE), lambda b, p, i: (b * nq + i, p)),
                            pl.BlockSpec((S, 2 * LANE), lambda b, p, i: (b, p)),
                            pl.BlockSpec((S, LANE), lambda b, p, i: (b, p))],
                  out_specs=[pl.BlockSpec((tq, LANE), lambda b, p, i: (b * nq + i, p)),
                             pl.BlockSpec((tq, LANE), lambda b, p, i: (b * nq + i, p))],
                  out_shape=[_sds((T, NH * NOPE), MX), _sds((T, NH * NOPE), F32)])(q, k, v)


def _attn_bwd(q, k, v, o, lse, do, *, nb, name, tq=256):
    T = q.shape[0]
    S = T // nb
    nq = S // tq
    tk = tq
    npair = NH // 2

    def body(q_ref, k_ref, v_ref, o_ref, lse_ref, do_ref, dq_ref, dk_ref, dv_ref):
        dk_ref[...] = jnp.zeros_like(dk_ref)
        dv_ref[...] = jnp.zeros_like(dv_ref)
        lane = lax.broadcasted_iota(jnp.int32, (tq, LANE), 1)
        for hh in range(2):
            mine = jnp.logical_and(lane >= hh * NOPE, lane < (hh + 1) * NOPE)

            def q_step(qi, _, hh=hh, mine=mine):
                q0 = pl.multiple_of(qi * tq, tq)
                qh = q_ref[pl.ds(q0, tq), hh * LANE:(hh + 1) * LANE]
                dof = jnp.where(mine, do_ref[pl.ds(q0, tq), :], 0.0)
                dob = dof.astype(MX)
                delta = jnp.sum(dof * o_ref[pl.ds(q0, tq), :].astype(F32), axis=-1, keepdims=True)
                lse_h = jnp.sum(jnp.where(lane == hh * NOPE, lse_ref[pl.ds(q0, tq), :], 0.0), axis=-1, keepdims=True)

                def k_step(kj, dq_acc):
                    k0 = pl.multiple_of(kj * tk, tk)
                    kb = k_ref[pl.ds(k0, tk), hh * LANE:(hh + 1) * LANE]
                    vb = v_ref[pl.ds(k0, tk), :]
                    s = lax.dot_general(qh, kb, (((1,), (1,)), ((), ())), preferred_element_type=F32) * ATTN_SCALE
                    p = jnp.where(_causal_mask(qi, kj, tq, tk), jnp.exp(s - lse_h), 0.0)
                    dp = lax.dot_general(dob, vb, (((1,), (1,)), ((), ())), preferred_element_type=F32)
                    ds = (p * (dp - delta) * ATTN_SCALE).astype(MX)
                    pb = p.astype(MX)
                    dk_ref[pl.ds(k0, tk), hh * LANE:(hh + 1) * LANE] += lax.dot_general(
                        ds, qh, (((0,), (0,)), ((), ())), preferred_element_type=F32)
                    dv_ref[pl.ds(k0, tk), :] += lax.dot_general(pb, dob, (((0,), (0,)), ((), ())), preferred_element_type=F32)
                    return dq_acc + jnp.dot(ds, kb, preferred_element_type=F32)

                dq_ref[pl.ds(q0, tq), hh * LANE:(hh + 1) * LANE] = lax.fori_loop(0, qi + 1, k_step, jnp.zeros((tq, LANE), F32))
                return 0

            lax.fori_loop(0, nq, q_step, 0)

    pair256 = pl.BlockSpec((S, 2 * LANE), lambda b, p: (b, p))
    pair128 = pl.BlockSpec((S, LANE), lambda b, p: (b, p))
    return _pcall(body, name=name, grid=(nb, npair),
                  in_specs=[pair256, pair256, pair128, pair128, pair128, pair128],
                  out_specs=[pair256, pair256, pair128],
                  out_shape=[_sds((T, NH * LANE), F32), _sds((T, NH * LANE), F32), _sds((T, NH * NOPE), F32)],
                  )(q, k, v, o, lse, do)


def _layer_fwd_bwd(x, tgt, mod3, cos, sin, W, P):
    nb = mod3.shape[0]
    h1 = _fwd_block(x, None, mod3, P["norm_ffn1"], sub=0, coef=0.0, name="fwd_norm1")
    gu1 = _mm(h1, W["ffn1_in"], mode="nn", out_dtype=F32, name="ffn1_in")
    a1 = _swiglu_fwd(gu1, name="ffn1_act")
    f1 = _mm(a1, W["ffn1_out"], mode="nn", out_dtype=F32, name="ffn1_out", tk=DFF)
    x1, h2 = _fwd_block(x, f1, mod3, P["norm_mix"], sub=1, coef=0.5, name="fwd_norm2")
    z = _mm(h2, W["w_in"], mode="nn", out_dtype=F32, name="mix_in", tn=1664)
    pooled, mixed, scaled = _pool_fwd(z, W["pool_grp"], P["pool_scale"], nb=nb, name="pool_fwd")
    br_pool = _mm(scaled, W["pool_proj"], mode="nn", out_dtype=F32, name="pool_proj")
    qn, kvn = _lat_fwd(z, P["q_a_norm"], P["kv_a_norm"], name="lat_fwd")
    qp = _mm(qn, W["q_up"], mode="nn", out_dtype=F32, name="q_up")
    kvp = _mm(kvn, W["kv_up"], mode="nn", out_dtype=F32, name="kv_up")
    q, k, v = _prep_fwd(qp, kvp, z, cos, sin, P["q_gain"], P["k_gain"], name="prep_fwd")
    attn, lse = _attn_fwd(q, k, v, nb=nb, name="attn_fwd")
    br_mla = _mm(attn, W["mla_proj"], mode="nn", out_dtype=F32, name="mla_proj")
    merged = _merge_fwd(z, br_pool, br_mla, name="merge_fwd")
    mo = _mm(merged, W["w_out"], mode="nn", out_dtype=F32, name="mix_out")
    x2, h3 = _fwd_block(x1, mo, mod3, P["norm_ffn2"], sub=2, coef=1.0, name="fwd_norm3")
    gu2 = _mm(h3, W["ffn2_in"], mode="nn", out_dtype=F32, name="ffn2_in")
    a2 = _swiglu_fwd(gu2, name="ffn2_act")
    f2 = _mm(a2, W["ffn2_out"], mode="nn", out_dtype=F32, name="ffn2_out", tk=DFF)
    dy, df2, st_fin, loss = _final(x2, f2, tgt, mod3, name="loss_head")

    G = {}
    G["ffn2_out"] = _mm(a2, df2, mode="tn", out_dtype=F32, name="d_ffn2_out")
    da2 = _mm(df2, W["ffn2_out"], mode="nt", out_dtype=F32, name="d_ffn2_act", tn=1408)
    dgu2 = _swiglu_bwd(gu2, da2, name="ffn2_act_bwd")
    G["ffn2_in"] = _mm(h3, dgu2, mode="tn", out_dtype=F32, name="d_ffn2_in")
    dh3 = _mm(dgu2, W["ffn2_in"], mode="nt", out_dtype=F32, name="d_ffn2_h", tk=DFF)
    dx2, dmo, st3 = _bwd_block(x2, dh3, dy, mo, mod3, P["norm_ffn2"], sub=2, coef=1.0, name="bwd_norm3")
    G["w_out"] = _mm(merged, dmo, mode="tn", out_dtype=F32, name="d_mix_out")
    dmerged = _mm(dmo, W["w_out"], mode="nt", out_dtype=F32, name="d_merged")
    dbr_pool, dbr_mla, dgates = _merge_bwd(z, br_pool, br_mla, dmerged, name="merge_bwd")
    G["pool_proj"] = _mm(scaled, dbr_pool, mode="tn", out_dtype=F32, name="d_pool_proj")
    dscaled = _mm(dbr_pool, W["pool_proj"], mode="nt", out_dtype=F32, name="d_pool_scaled")
    du_pool, d_pool_grp, d_pool_scale = _pool_bwd(dscaled, mixed, pooled, W["pool_grp"], P["pool_scale"], nb=nb, name="pool_bwd")
    G["mla_proj"] = _mm(attn, dbr_mla, mode="tn", out_dtype=F32, name="d_mla_proj")
    dattn = _mm(dbr_mla, W["mla_proj"], mode="nt", out_dtype=F32, name="d_attn")
    dq, dk, dv = _attn_bwd(q, k, v, attn, lse, dattn, nb=nb, name="attn_bwd")
    dqp, dkvp, dkr, st_prep = _prep_bwd(dq, dk, dv, qp, kvp, z, cos, sin, P["q_gain"], P["k_gain"], nb=nb, name="prep_bwd")
    G["q_up"] = _mm(qn, dqp, mode="tn", out_dtype=F32, name="d_q_up")
    dqn = _mm(dqp, W["q_up"], mode="nt", out_dtype=F32, name="d_qn")
    G["kv_up"] = _mm(kvn, dkvp, mode="tn", out_dtype=F32, name="d_kv_up")
    dkvn = _mm(dkvp, W["kv_up"], mode="nt", out_dtype=F32, name="d_kvn", tk=1536)
    dql, dkvl, st_q, st_kv = _lat_bwd(z, dqn, dkvn, P["q_a_norm"], P["kv_a_norm"], nb=nb, name="lat_bwd")
    dz = jnp.concatenate([du_pool, dql, dkvl, dkr, dgates], axis=1)
    G["w_in"] = _mm(h2, dz, mode="tn", out_dtype=F32, name="d_mix_in", tn=1664)
    dh2 = _mm(dz, W["w_in"], mode="nt", out_dtype=F32, name="d_mix_h", tk=1664)
    dx1, df1, st2 = _bwd_block(x1, dh2, dx2, f1, mod3, P["norm_mix"], sub=1, coef=0.5, name="bwd_norm2")
    G["ffn1_out"] = _mm(a1, df1, mode="tn", out_dtype=F32, name="d_ffn1_out")
    da1 = _mm(df1, W["ffn1_out"], mode="nt", out_dtype=F32, name="d_ffn1_act", tn=1408)
    dgu1 = _swiglu_bwd(gu1, da1, name="ffn1_act_bwd")
    G["ffn1_in"] = _mm(h1, dgu1, mode="tn", out_dtype=F32, name="d_ffn1_in")
    dh1 = _mm(dgu1, W["ffn1_in"], mode="nt", out_dtype=F32, name="d_ffn1_h", tk=DFF)
    grad_x, st1 = _bwd_block(x, dh1, dx1, None, mod3, P["norm_ffn1"], sub=0, coef=0.0, name="bwd_norm1")

    dmod = jnp.stack([st1[:, 0], st1[:, 1], st2[:, 3], st2[:, 0], st2[:, 1], st3[:, 3], st3[:, 0], st3[:, 1], st_fin[:, 0]],
                     axis=1)
    small = {
        "norm_ffn1": st1[:, 2], "norm_mix": st2[:, 2], "norm_ffn2": st3[:, 2],
        "pool_grp": d_pool_grp, "pool_scale": d_pool_scale[:, 0],
        "q_a_norm": st_q[:, 0], "kv_a_norm": st_kv[:, 0],
        "q_gain": st_prep[:, 0], "k_gain": st_prep[:, 1],
    }
    return loss, grad_x, G, small, dmod


def _w_in_to_kernel(w):
    k = w.shape[0]
    zeros = lambda n: jnp.zeros((k, n), w.dtype)
    return jnp.concatenate([w[:, 0:1152], zeros(KR_LANE), w[:, 1152:1184], zeros(LANE - KR_LANE - ROPE), w[:, 1184:]], axis=1)


def _w_in_from_kernel(g):
    return jnp.concatenate([g[:, 0:1152], g[:, Z_KR + KR_LANE:Z_KR + KR_LANE + ROPE], g[:, Z_GP:]], axis=1)


def _q_up_to_kernel(w):
    k = w.shape[0]
    return jnp.pad(w.reshape(k, NH, NOPE + ROPE), ((0, 0), (0, 0), (0, LANE - NOPE - ROPE))).reshape(k, NH * LANE)


def _q_up_from_kernel(g):
    k = g.shape[0]
    return g.reshape(k, NH, LANE)[:, :, :NOPE + ROPE].reshape(k, NH * (NOPE + ROPE))


def _kv_up_to_kernel(w):
    k = w.shape[0]
    w3 = w.reshape(k, NH, 2 * NOPE)
    kpart = jnp.pad(w3[:, :, :NOPE], ((0, 0), (0, 0), (0, LANE - NOPE))).reshape(k, NH * LANE)
    return jnp.concatenate([kpart, w3[:, :, NOPE:].reshape(k, NH * NOPE)], axis=1)


def _kv_up_from_kernel(g):
    k = g.shape[0]
    kpart = g[:, :NH * LANE].reshape(k, NH, LANE)[:, :, :NOPE]
    vpart = g[:, NH * LANE:].reshape(k, NH, NOPE)
    return jnp.concatenate([kpart, vpart], axis=2).reshape(k, NH * 2 * NOPE)


def _gain_slab(nope, rope):
    return jnp.concatenate([nope, rope, jnp.zeros((1, LANE - NOPE - ROPE), nope.dtype)], axis=1)


def _rope_tables(positions):
    inv_freq = 10000.0 ** (-jnp.arange(0, ROPE, 2, dtype=F32) / ROPE)
    ang = positions.astype(F32)[:, None] * inv_freq
    ang = jnp.concatenate([ang, ang], axis=-1)
    t = positions.shape[0]
    cos = jnp.concatenate([jnp.ones((t, KR_LANE), F32), jnp.cos(ang), jnp.ones((t, LANE - KR_LANE - ROPE), F32)], axis=1)
    sin = jnp.concatenate([jnp.zeros((t, KR_LANE), F32), jnp.sin(ang), jnp.zeros((t, LANE - KR_LANE - ROPE), F32)], axis=1)
    return cos, sin


def _coords():
    return lax.axis_index("x"), lax.axis_index("y"), lax.axis_index("c")


HBM_SPEC = pl.BlockSpec(memory_space=pl.ANY)
VMEM_SPEC = pl.BlockSpec(memory_space=pltpu.VMEM)


def _gather8(v, *, name):
    r, c = v.shape

    def body(v_ref, out_ref, send_sems, recv_sems, local_sem):
        x, y, cc = _coords()
        me = 4 * x + 2 * y + cc
        mine = pltpu.make_async_copy(v_ref, out_ref.at[me], local_sem)
        mine.start()
        copies = []
        for kk in range(1, N_DEV):
            peer = (x ^ (kk >> 2), y ^ ((kk >> 1) & 1), cc ^ (kk & 1))
            cp = pltpu.make_async_remote_copy(src_ref=v_ref, dst_ref=out_ref.at[me], send_sem=send_sems.at[kk - 1],
                                              recv_sem=recv_sems.at[kk - 1], device_id=peer, device_id_type=MESH)
            cp.start()
            copies.append(cp)
        for kk in range(1, N_DEV):
            peer_slot = me ^ kk
            pltpu.make_async_remote_copy(src_ref=v_ref, dst_ref=out_ref.at[peer_slot], send_sem=send_sems.at[kk - 1],
                                         recv_sem=recv_sems.at[kk - 1], device_id=(x, y, cc), device_id_type=MESH).wait_recv()
        for cp in copies:
            cp.wait_send()
        mine.wait()

    return _pcall(body, name=name, out_shape=_sds((N_DEV, r, c), v.dtype), in_specs=[VMEM_SPEC], out_specs=VMEM_SPEC,
                  scratch=[pltpu.SemaphoreType.DMA((N_DEV - 1,)), pltpu.SemaphoreType.DMA((N_DEV - 1,)), pltpu.SemaphoreType.DMA])(v)


CHIP_RELS = ((1, 0), (0, 1), (1, 1))


def _gather_weights(shards, *, name):
    n = len(shards)

    def body(*refs):
        ins, outs = refs[:n], refs[n:2 * n]
        isend, irecv, dsend, drecv, lsem = refs[2 * n:]
        x, y, cc = _coords()
        chip = 2 * x + y
        sib = (x, y, 1 - cc)
        local_copies, sends = [], []
        for a in range(n):
            half = ins[a].shape[0] // 2
            mine = pl.ds(cc * half, half)
            loc = pltpu.make_async_copy(ins[a], outs[a].at[chip], lsem.at[a])
            loc.start()
            local_copies.append(loc)
            for j, (dx, dy) in enumerate(CHIP_RELS):
                cp = pltpu.make_async_remote_copy(src_ref=ins[a].at[mine], dst_ref=outs[a].at[chip, mine],
                                                  send_sem=isend.at[a, j], recv_sem=irecv.at[a, j],
                                                  device_id=(x ^ dx, y ^ dy, cc), device_id_type=MESH)
                cp.start()
                sends.append(cp)
        for a in range(n):
            half = ins[a].shape[0] // 2
            mine = pl.ds(cc * half, half)
            for j, (dx, dy) in enumerate(CHIP_RELS):
                src_chip = 2 * (x ^ dx) + (y ^ dy)
                landed = outs[a].at[src_chip, mine]
                pltpu.make_async_remote_copy(src_ref=landed, dst_ref=landed, send_sem=isend.at[a, j], recv_sem=irecv.at[a, j],
                                             device_id=sib, device_id_type=MESH).wait_recv()
                fwd = pltpu.make_async_remote_copy(src_ref=landed, dst_ref=landed, send_sem=dsend.at[a, j],
                                                   recv_sem=drecv.at[a, j], device_id=sib, device_id_type=MESH)
                fwd.start()
                sends.append(fwd)
        for a in range(n):
            half = ins[a].shape[0] // 2
            theirs = pl.ds((1 - cc) * half, half)
            for j, (dx, dy) in enumerate(CHIP_RELS):
                src_chip = 2 * (x ^ dx) + (y ^ dy)
                landed = outs[a].at[src_chip, theirs]
                pltpu.make_async_remote_copy(src_ref=landed, dst_ref=landed, send_sem=dsend.at[a, j], recv_sem=drecv.at[a, j],
                                             device_id=sib, device_id_type=MESH).wait_recv()
        for cp in sends:
            cp.wait_send()
        for loc in local_copies:
            loc.wait()

    return _pcall(body, name=name, out_shape=[_sds((N_CHIP,) + s.shape, s.dtype) for s in shards],
                  in_specs=[HBM_SPEC] * n, out_specs=[HBM_SPEC] * n,
                  scratch=[pltpu.SemaphoreType.DMA((n, 3)), pltpu.SemaphoreType.DMA((n, 3)), pltpu.SemaphoreType.DMA((n, 3)),
                           pltpu.SemaphoreType.DMA((n, 3)), pltpu.SemaphoreType.DMA((n,))])(*shards)


def _swap_halves(parts, *, name):
    n = len(parts)

    def body(*refs):
        ins, outs = refs[:n], refs[n:2 * n]
        ssem, rsem = refs[2 * n:]
        x, y, cc = _coords()
        sib = (x, y, 1 - cc)
        cps = []
        for a in range(n):
            half = ins[a].shape[1] // 2
            cp = pltpu.make_async_remote_copy(src_ref=ins[a].at[:, pl.ds((1 - cc) * half, half)], dst_ref=outs[a],
                                              send_sem=ssem.at[a], recv_sem=rsem.at[a], device_id=sib, device_id_type=MESH)
            cp.start()
            cps.append(cp)
        for cp in cps:
            cp.wait()

    return _pcall(body, name=name,
                  out_shape=[_sds((p.shape[0], p.shape[1] // 2, p.shape[2]), p.dtype) for p in parts],
                  in_specs=[HBM_SPEC] * n, out_specs=[HBM_SPEC] * n,
                  scratch=[pltpu.SemaphoreType.DMA((n,)), pltpu.SemaphoreType.DMA((n,))])(*parts)


def _add_half(full, other, cidx, *, name):
    nch, r, c = full.shape
    half = r // 2
    tr = _pick_rows(half)
    nbk = half // tr
    grid_spec = pltpu.PrefetchScalarGridSpec(
        num_scalar_prefetch=1, grid=(nch, nbk),
        in_specs=[pl.BlockSpec((1, tr, c), lambda j, i, cref: (j, cref[0] * nbk + i, 0)),
                  pl.BlockSpec((1, tr, c), lambda j, i, cref: (j, i, 0))],
        out_specs=pl.BlockSpec((1, tr, c), lambda j, i, cref: (j, i, 0)))

    def body(cref, a_ref, b_ref, o_ref):
        o_ref[...] = (a_ref[...] + b_ref[...]).astype(o_ref.dtype)

    return _pcall(body, name=name, out_shape=_sds((nch, half, c), MX), grid_spec=grid_spec)(cidx, full, other)


def _pick_rows(rows, target=512):
    best = None
    for t in range(16, min(rows, target) + 1, 16):
        if rows % t == 0:
            best = t
    return rows if best is None else best


def _exchange_chips(parts, *, name):
    n = len(parts)

    def body(*refs):
        ins, outs = refs[:n], refs[n:2 * n]
        ssem, rsem, lsem = refs[2 * n:]
        x, y, cc = _coords()
        chip = 2 * x + y
        cps = []
        for a in range(n):
            loc = pltpu.make_async_copy(ins[a].at[chip], outs[a].at[chip], lsem.at[a])
            loc.start()
            cps.append((loc, None))
            for j, (dx, dy) in enumerate(CHIP_RELS):
                peer_chip = 2 * (x ^ dx) + (y ^ dy)
                cp = pltpu.make_async_remote_copy(src_ref=ins[a].at[peer_chip], dst_ref=outs[a].at[chip],
                                                  send_sem=ssem.at[a, j], recv_sem=rsem.at[a, j],
                                                  device_id=(x ^ dx, y ^ dy, cc), device_id_type=MESH)
                cp.start()
                cps.append((cp, peer_chip))
        for a in range(n):
            for j, (dx, dy) in enumerate(CHIP_RELS):
                peer_chip = 2 * (x ^ dx) + (y ^ dy)
                pltpu.make_async_remote_copy(src_ref=ins[a].at[peer_chip], dst_ref=outs[a].at[peer_chip], send_sem=ssem.at[a, j],
                                             recv_sem=rsem.at[a, j], device_id=(x, y, cc), device_id_type=MESH).wait_recv()
        for cp, peer_chip in cps:
            if peer_chip is None:
                cp.wait()
            else:
                cp.wait_send()

    return _pcall(body, name=name, out_shape=[_sds(p.shape, p.dtype) for p in parts],
                  in_specs=[HBM_SPEC] * n, out_specs=[HBM_SPEC] * n,
                  scratch=[pltpu.SemaphoreType.DMA((n, 3)), pltpu.SemaphoreType.DMA((n, 3)), pltpu.SemaphoreType.DMA((n,))])(*parts)


def _sum_chips(q, *, name):
    nch, h, c = q.shape
    tr = _pick_rows(h)

    def body(q_ref, o_ref):
        acc = q_ref[0].astype(F32) + q_ref[1].astype(F32)
        acc = acc + q_ref[2].astype(F32)
        o_ref[...] = acc + q_ref[3].astype(F32)

    return _pcall(body, name=name, out_shape=_sds((h, c), F32), grid=(h // tr,),
                  in_specs=[pl.BlockSpec((nch, tr, c), lambda i: (0, i, 0))], out_specs=pl.BlockSpec((tr, c), lambda i: (i, 0)))(q)


def _join_halves(halves, *, name):
    n = len(halves)

    def body(*refs):
        ins, outs = refs[:n], refs[n:2 * n]
        ssem, rsem, lsem = refs[2 * n:]
        x, y, cc = _coords()
        sib = (x, y, 1 - cc)
        cps, locs = [], []
        for a in range(n):
            h = ins[a].shape[0]
            mine = pl.ds(cc * h, h)
            loc = pltpu.make_async_copy(ins[a], outs[a].at[mine], lsem.at[a])
            loc.start()
            locs.append(loc)
            cp = pltpu.make_async_remote_copy(src_ref=ins[a], dst_ref=outs[a].at[mine], send_sem=ssem.at[a], recv_sem=rsem.at[a],
                                              device_id=sib, device_id_type=MESH)
            cp.start()
            cps.append(cp)
        for a in range(n):
            h = ins[a].shape[0]
            theirs = outs[a].at[pl.ds((1 - cc) * h, h)]
            pltpu.make_async_remote_copy(src_ref=ins[a], dst_ref=theirs, send_sem=ssem.at[a], recv_sem=rsem.at[a],
                                         device_id=sib, device_id_type=MESH).wait_recv()
        for cp in cps:
            cp.wait_send()
        for loc in locs:
            loc.wait()

    return _pcall(body, name=name, out_shape=[_sds((2 * p.shape[0], p.shape[1]), p.dtype) for p in halves],
                  in_specs=[HBM_SPEC] * n, out_specs=[HBM_SPEC] * n,
                  scratch=[pltpu.SemaphoreType.DMA((n,)), pltpu.SemaphoreType.DMA((n,)), pltpu.SemaphoreType.DMA((n,))])(*halves)


def _ada_fwd(c_all, w, b, *, name):
    m, kdim = c_all.shape
    n = w.shape[1]
    tn = _pick(n, 1152)

    def body(c_ref, w_ref, b_ref, o_ref):
        cv = c_ref[...]
        act = (cv * _sigmoid(cv)).astype(MX)
        o_ref[...] = jnp.dot(act, w_ref[...].astype(MX), preferred_element_type=F32) + b_ref[...]

    return _pcall(body, name=name, out_shape=_sds((m, n), F32), grid=(n // tn,),
                  in_specs=[pl.BlockSpec((m, kdim), lambda j: (0, 0)), pl.BlockSpec((kdim, tn), lambda j: (0, j)),
                            pl.BlockSpec((1, tn), lambda j: (0, j))],
                  out_specs=pl.BlockSpec((m, tn), lambda j: (0, j)))(c_all, w, b)


def _ada_bwd(c_all, dmod_cols, *, name):
    m, kdim = c_all.shape
    n = dmod_cols.shape[1]
    tn = _pick(n, 1152)

    def body(c_ref, d_ref, o_ref):
        cv = c_ref[...]
        act = (cv * _sigmoid(cv)).astype(MX)
        o_ref[...] = lax.dot_general(act, d_ref[...].astype(MX), (((0,), (0,)), ((), ())), preferred_element_type=F32)

    return _pcall(body, name=name, out_shape=_sds((kdim, n), F32), grid=(n // tn,),
                  in_specs=[pl.BlockSpec((m, kdim), lambda j: (0, 0)), pl.BlockSpec((m, tn), lambda j: (0, j))],
                  out_specs=pl.BlockSpec((kdim, tn), lambda j: (0, j)))(c_all, dmod_cols)


def _sum_lead(v, *, name):
    n, r, c = v.shape

    def body(v_ref, o_ref):
        acc = v_ref[0]
        for i in range(1, n):
            acc = acc + v_ref[i]
        o_ref[...] = acc

    return _pcall(body, name=name, out_shape=_sds((r, c), F32), in_specs=[VMEM_SPEC], out_specs=VMEM_SPEC)(v)


def _adamw(w, g, m, v, *, name):
    r, c = w.shape
    tr = _pick_rows(r, 256)
    spec = pl.BlockSpec((tr, c), lambda i: (i, 0))
    bc1 = 1.0 / (1.0 - ADAM_B1 ** ADAM_STEP)
    bc2 = 1.0 / (1.0 - ADAM_B2 ** ADAM_STEP)

    def body(w_ref, g_ref, m_ref, v_ref, d_ref, mo_ref, vo_ref):
        gv = g_ref[...]
        mn = ADAM_B1 * m_ref[...] + (1.0 - ADAM_B1) * gv
        vn = ADAM_B2 * v_ref[...] + (1.0 - ADAM_B2) * (gv * gv)
        mo_ref[...] = mn
        vo_ref[...] = vn
        d_ref[...] = -ADAM_LR * ((mn * bc1) / (jnp.sqrt(vn * bc2) + ADAM_EPS) + ADAM_WD * w_ref[...])

    out = _sds((r, c), F32)
    return _pcall(body, name=name, out_shape=[out, out, out], grid=(r // tr,), in_specs=[spec] * 4, out_specs=[spec] * 3)(w, g, m, v)


BIG = ("w_ffn1_in", "w_ffn1_out", "w_in", "w_pool_proj", "w_q_up", "w_kv_up", "w_mla_proj", "w_out", "w_ffn2_in", "w_ffn2_out")
ROW_SHARDED = ("w_ffn1_out", "w_out", "w_ffn2_out")
KERNEL_NAME = {"w_ffn1_in": "ffn1_in", "w_ffn1_out": "ffn1_out", "w_in": "w_in", "w_pool_proj": "pool_proj", "w_q_up": "q_up",
               "w_kv_up": "kv_up", "w_mla_proj": "mla_proj", "w_out": "w_out", "w_ffn2_in": "ffn2_in", "w_ffn2_out": "ffn2_out"}
WEIGHTS = ("w_ada", "b_ada", "norm_ffn1", "w_ffn1_in", "w_ffn1_out", "norm_mix", "w_in", "pool_grp", "pool_scale", "w_pool_proj",
           "q_a_norm", "w_q_up", "kv_a_norm", "w_kv_up", "q_norm_nope", "q_norm_rope", "k_norm_nope", "k_norm_rope", "w_mla_proj",
           "w_out", "norm_ffn2", "w_ffn2_in", "w_ffn2_out")
SMALL = ("b_ada", "norm_ffn1", "norm_mix", "pool_grp", "pool_scale", "q_a_norm", "kv_a_norm", "q_norm_nope", "q_norm_rope",
         "k_norm_nope", "k_norm_rope", "norm_ffn2")
SLAB_W = 1024


def _assemble(name, stacked):
    if name in ROW_SHARDED:
        return stacked.reshape(stacked.shape[0] * stacked.shape[1], stacked.shape[2])
    return jnp.transpose(stacked, (1, 0, 2)).reshape(stacked.shape[1], stacked.shape[0] * stacked.shape[2])


def _split(name, full):
    if name in ROW_SHARDED:
        return full.reshape(N_CHIP, full.shape[0] // N_CHIP, full.shape[1])
    return jnp.transpose(full.reshape(full.shape[0], N_CHIP, full.shape[1] // N_CHIP), (1, 0, 2))


def _to_rows(v, width=SLAB_W):
    flat = v.reshape(-1)
    rows = -(-flat.shape[0] // width)
    return jnp.pad(flat, (0, rows * width - flat.shape[0])).reshape(rows, width)


def _pack_small(parts):
    rows, spans, at = [], {}, 0
    for name in SMALL:
        r = _to_rows(parts[name])
        spans[name] = (at, parts[name].size, parts[name].shape)
        rows.append(r)
        at += r.shape[0]
    pad = (-at) % SUBLANE
    if pad:
        rows.append(jnp.zeros((pad, SLAB_W), F32))
    return jnp.concatenate(rows, axis=0), spans


def _unpack_small(slab, spans):
    out = {}
    for name, (at, size, shape) in spans.items():
        nrow = -(-size // SLAB_W)
        out[name] = slab[at:at + nrow].reshape(-1)[:size].reshape(shape)
    return out


def kernel(x, c, positions, w_ada, b_ada, norm_ffn1, w_ffn1_in, w_ffn1_out, norm_mix, w_in, pool_grp, pool_scale, w_pool_proj, q_a_norm, w_q_up, kv_a_norm, w_kv_up, q_norm_nope, q_norm_rope, k_norm_nope, k_norm_rope, w_mla_proj, w_out, norm_ffn2, w_ffn2_in, w_ffn2_out, loss_target, m_w_ada, m_b_ada, m_norm_ffn1, m_w_ffn1_in, m_w_ffn1_out, m_norm_mix, m_w_in, m_pool_grp, m_pool_scale, m_w_pool_proj, m_q_a_norm, m_w_q_up, m_kv_a_norm, m_w_kv_up, m_q_norm_nope, m_q_norm_rope, m_k_norm_nope, m_k_norm_rope, m_w_mla_proj, m_w_out, m_norm_ffn2, m_w_ffn2_in, m_w_ffn2_out, v_w_ada, v_b_ada, v_norm_ffn1, v_w_ffn1_in, v_w_ffn1_out, v_norm_mix, v_w_in, v_pool_grp, v_pool_scale, v_w_pool_proj, v_q_a_norm, v_w_q_up, v_kv_a_norm, v_w_kv_up, v_q_norm_nope, v_q_norm_rope, v_k_norm_nope, v_k_norm_rope, v_w_mla_proj, v_w_out, v_norm_ffn2, v_w_ffn2_in, v_w_ffn2_out):
    args = dict(locals())
    wts = {n: args[n][0] for n in WEIGHTS}
    mom = {n: args["m_" + n][0] for n in WEIGHTS}
    var = {n: args["v_" + n][0] for n in WEIGHTS}
    nb, seq, dm = x.shape
    tokens = nb * seq
    xi, yi, ci = _coords()
    chip = 2 * xi + yi
    dev = 4 * xi + 2 * yi + ci

    c_all = _gather8(c, name="gather_c").reshape(N_DEV * nb, dm)
    ncol = w_ada.shape[2]
    b_cols = lax.dynamic_slice_in_dim(wts["b_ada"].reshape(1, -1), chip * ncol, ncol, axis=1)
    mod_cols = _ada_fwd(c_all, wts["w_ada"], b_cols, name="ada_fwd")
    mod_all = _gather8(mod_cols, name="gather_mod")
    mine = lax.dynamic_slice_in_dim(mod_all, dev * nb, nb, axis=1)
    mod = jnp.concatenate([mine[0], mine[2], mine[4], mine[6]], axis=1)
    mod3 = mod.reshape(nb, 9, dm)

    gathered = _gather_weights([wts[n].astype(MX) for n in BIG], name="gather_weights")
    full = {n: _assemble(n, g) for n, g in zip(BIG, gathered)}
    W = {KERNEL_NAME[n]: full[n] for n in BIG}
    W["w_in"] = _w_in_to_kernel(full["w_in"])
    W["q_up"] = _q_up_to_kernel(full["w_q_up"])
    W["kv_up"] = _kv_up_to_kernel(full["w_kv_up"])
    W["pool_grp"] = wts["pool_grp"].astype(MX)
    P = {"norm_ffn1": wts["norm_ffn1"].reshape(1, dm), "norm_mix": wts["norm_mix"].reshape(1, dm),
         "norm_ffn2": wts["norm_ffn2"].reshape(1, dm), "pool_scale": wts["pool_scale"].reshape(1, POOL_W),
         "q_a_norm": wts["q_a_norm"].reshape(1, QL), "kv_a_norm": wts["kv_a_norm"].reshape(1, KVL),
         "q_gain": _gain_slab(wts["q_norm_nope"].reshape(1, NOPE), wts["q_norm_rope"].reshape(1, ROPE)),
         "k_gain": _gain_slab(wts["k_norm_nope"].reshape(1, NOPE), wts["k_norm_rope"].reshape(1, ROPE))}
    cos, sin = _rope_tables(positions.reshape(tokens))

    loss8, grad_x, G, small, dmod = _layer_fwd_bwd(x.reshape(tokens, dm), loss_target.reshape(tokens, dm), mod3, cos, sin, W, P)
    loss = lax.psum(loss8[0, 0], ("x", "y", "c"))

    gfull = {KERNEL_NAME[n]: G[KERNEL_NAME[n]] for n in BIG}
    gfull["w_in"] = _w_in_from_kernel(G["w_in"])
    gfull["q_up"] = _q_up_from_kernel(G["q_up"])
    gfull["kv_up"] = _kv_up_from_kernel(G["kv_up"])
    parts = [_split(n, gfull[KERNEL_NAME[n]]) for n in BIG]
    cidx = ci.astype(jnp.int32).reshape(1)
    from_sib = _swap_halves(parts, name="rs_swap_halves")
    pre = [_add_half(p, o, cidx, name="rs_add_" + KERNEL_NAME[n]) for n, p, o in zip(BIG, parts, from_sib)]
    landed = _exchange_chips(pre, name="rs_exchange_chips")
    halves = [_sum_chips(q, name="rs_sum_" + KERNEL_NAME[n]) for n, q in zip(BIG, landed)]
    joined = _join_halves(halves, name="rs_join_halves")
    grads = {n: g for n, g in zip(BIG, joined)}

    qg, kg = small["q_gain"], small["k_gain"]
    per_seq = {"b_ada": dmod.reshape(nb, 9 * dm), "norm_ffn1": small["norm_ffn1"], "norm_mix": small["norm_mix"],
               "pool_grp": small["pool_grp"], "pool_scale": small["pool_scale"], "q_a_norm": small["q_a_norm"],
               "kv_a_norm": small["kv_a_norm"], "q_norm_nope": qg[:, :NOPE], "q_norm_rope": qg[:, NOPE:NOPE + ROPE],
               "k_norm_nope": kg[:, :NOPE], "k_norm_rope": kg[:, KR_LANE:KR_LANE + ROPE], "norm_ffn2": small["norm_ffn2"]}
    slabs, spans = [], None
    for s in range(nb):
        slab, spans = _pack_small({n: per_seq[n][s] for n in SMALL})
        slabs.append(slab)
    rows = slabs[0].shape[0]
    gathered_small = _gather8(jnp.concatenate(slabs, axis=0), name="gather_small")
    small_sum = _sum_lead(gathered_small.reshape(N_DEV * nb, rows, SLAB_W), name="sum_small")
    small_grads = _unpack_small(small_sum, spans)
    for n in SMALL:
        grads[n] = small_grads[n].reshape(wts[n].shape)
    at, _, _ = spans["b_ada"]
    nrow_b = 9 * dm // SLAB_W
    dmod_all = gathered_small.reshape(N_DEV * nb, rows, SLAB_W)[:, at:at + nrow_b].reshape(N_DEV * nb, 9 * dm)
    dmod_cols = lax.dynamic_slice_in_dim(dmod_all, chip * ncol, ncol, axis=1)
    grads["w_ada"] = _ada_bwd(c_all, dmod_cols, name="ada_bwd")

    delta, new_m, new_v = {}, {}, {}
    for n in ("w_ada",) + BIG:
        delta[n], new_m[n], new_v[n] = _adamw(wts[n], grads[n], mom[n], var[n], name="adamw_" + n)
    w_slab, sp = _pack_small({n: wts[n] for n in SMALL})
    g_slab, _ = _pack_small({n: grads[n] for n in SMALL})
    m_slab, _ = _pack_small({n: mom[n] for n in SMALL})
    v_slab, _ = _pack_small({n: var[n] for n in SMALL})
    d_s, m_s, v_s = _adamw(w_slab, g_slab, m_slab, v_slab, name="adamw_small")
    for dst, slab in ((delta, d_s), (new_m, m_s), (new_v, v_s)):
        un = _unpack_small(slab, sp)
        for n in SMALL:
            dst[n] = un[n]

    def lead(a, n):
        return a.reshape((1,) + wts[n].shape)

    return (loss, grad_x.reshape(nb, seq, dm), *[lead(grads[n], n) for n in WEIGHTS], *[lead(delta[n], n) for n in WEIGHTS],
            *[lead(new_m[n], n) for n in WEIGHTS], *[lead(new_v[n], n) for n in WEIGHTS])
```

```python
import functools
import math

import jax
import jax.numpy as jnp
from jax import lax
from jax.experimental import pallas as pl
from jax.experimental.pallas import tpu as pltpu

F32 = jnp.float32
MX = jnp.bfloat16

D = 1024
DFF = 2816
NH = 8
POOL_W = 512
POOL_G = 4
QL = 384
KVL = 256
ROPE = 32
NOPE = 64
LANE = 128
SUBLANE = 8
EPS = 1e-6
ATTN_SCALE = 1.0 / math.sqrt(96.0)
NEG = -1e30

Z_UP, Z_QL, Z_KV, Z_KR, Z_GP, Z_GM, Z_W = 0, 512, 896, 1152, 1280, 2304, 3328
KR_LANE = 64
LAT_W = 1280

ADAM_LR, ADAM_B1, ADAM_B2, ADAM_EPS, ADAM_WD, ADAM_STEP = 0.001, 0.9, 0.999, 1e-08, 0.01, 10

VMEM_LIMIT = 48 * 1024 * 1024
MESH = pl.DeviceIdType.MESH
N_DEV = 8
N_CHIP = 4


def _pcall(body, *, name, out_shape, grid=(), in_specs=None, out_specs=None, scratch=(), grid_spec=None, aliases=None):
    params = pltpu.CompilerParams(vmem_limit_bytes=VMEM_LIMIT)
    kw = dict(name=name, out_shape=out_shape, compiler_params=params)
    if aliases:
        kw["input_output_aliases"] = aliases
    if grid_spec is not None:
        return pl.pallas_call(body, grid_spec=grid_spec, **kw)
    return pl.pallas_call(body, grid=grid, in_specs=in_specs, out_specs=out_specs, scratch_shapes=scratch, **kw)


def _pick(dim, target):
    best = None
    for t in range(LANE, min(dim, target) + 1, LANE):
        if dim % t == 0:
            best = t
    return dim if best is None else best


def _sds(shape, dtype):
    return jax.ShapeDtypeStruct(shape, dtype)


def _mm(a, b, *, mode, out_dtype, name, tm=512, tn=512, tk=1024, n_outer=False):
    if mode == "nn":
        (M, K), (K2, N) = a.shape, b.shape
    elif mode == "nt":
        (M, K), (N, K2) = a.shape, b.shape
    else:
        (K, M), (K2, N) = a.shape, b.shape
    assert K == K2, (name, a.shape, b.shape)
    tm, tn, tk = _pick(M, tm), _pick(N, tn), _pick(K, tk)
    nk = K // tk
    if n_outer:
        ij = lambda g0, g1: (g1, g0)
        grid = (N // tn, M // tm, nk)
    else:
        ij = lambda g0, g1: (g0, g1)
        grid = (M // tm, N // tn, nk)
    if mode == "tn":
        a_spec = pl.BlockSpec((tk, tm), lambda g0, g1, k: (k, ij(g0, g1)[0]))
    else:
        a_spec = pl.BlockSpec((tm, tk), lambda g0, g1, k: (ij(g0, g1)[0], k))
    if mode == "nt":
        b_spec = pl.BlockSpec((tn, tk), lambda g0, g1, k: (ij(g0, g1)[1], k))
    else:
        b_spec = pl.BlockSpec((tk, tn), lambda g0, g1, k: (k, ij(g0, g1)[1]))
    o_spec = pl.BlockSpec((tm, tn), lambda g0, g1, k: ij(g0, g1))
    dn = {"nn": (((1,), (0,)), ((), ())), "nt": (((1,), (1,)), ((), ())), "tn": (((0,), (0,)), ((), ()))}[mode]

    def dot(a_ref, b_ref):
        return lax.dot_general(a_ref[...].astype(MX), b_ref[...].astype(MX), dn, preferred_element_type=F32)

    def body_one(a_ref, b_ref, o_ref):
        o_ref[...] = dot(a_ref, b_ref).astype(o_ref.dtype)

    def body_acc(a_ref, b_ref, o_ref, acc_ref):
        k = pl.program_id(2)
        part = dot(a_ref, b_ref)

        @pl.when(k == 0)
        def _():
            acc_ref[...] = part

        @pl.when(k > 0)
        def _():
            acc_ref[...] += part

        @pl.when(k == nk - 1)
        def _():
            o_ref[...] = acc_ref[...].astype(o_ref.dtype)

    return _pcall(body_one if nk == 1 else body_acc, name=name, out_shape=_sds((M, N), out_dtype), grid=grid,
                  in_specs=[a_spec, b_spec], out_specs=o_spec, scratch=[] if nk == 1 else [pltpu.VMEM((tm, tn), F32)])(a, b)


def _ffn_up(h, w_gu, *, name, tm=2048):
    T, dm = h.shape
    nchunk = w_gu.shape[1] // (2 * LANE)
    tm = _pick(T, tm)

    def body(h_ref, w_ref, gu_ref, a_ref):
        gu = jnp.dot(h_ref[...], w_ref[...], preferred_element_type=F32)
        g = gu[:, :LANE]
        gu_ref[...] = gu.astype(gu_ref.dtype)
        a_ref[...] = (g * _sigmoid(g) * gu[:, LANE:]).astype(a_ref.dtype)

    return _pcall(body, name=name, grid=(T // tm, nchunk),
                  in_specs=[pl.BlockSpec((tm, dm), lambda i, j: (i, 0)), pl.BlockSpec((dm, 2 * LANE), lambda i, j: (0, j))],
                  out_specs=[pl.BlockSpec((tm, 2 * LANE), lambda i, j: (i, j)), pl.BlockSpec((tm, LANE), lambda i, j: (i, j))],
                  out_shape=[_sds((T, w_gu.shape[1]), MX), _sds((T, nchunk * LANE), MX)])(h, w_gu)


def _ffn_bwd(df, gu, w_out, w_gu, *, name, tm=1024):
    T, dm = df.shape
    nchunk = gu.shape[1] // (2 * LANE)
    tm = _pick(T, tm)

    def body(df_ref, gu_ref, wo_ref, wg_ref, dgu_ref, dh_ref, acc_ref):
        j = pl.program_id(1)
        da = lax.dot_general(df_ref[...], wo_ref[...], (((1,), (1,)), ((), ())), preferred_element_type=F32)
        g = gu_ref[:, :LANE].astype(F32)
        u = gu_ref[:, LANE:].astype(F32)
        s = _sigmoid(g)
        dgu = jnp.concatenate([da * u * (s * (1.0 + g * (1.0 - s))), da * (g * s)], axis=1).astype(MX)
        dgu_ref[...] = dgu
        part = lax.dot_general(dgu, wg_ref[...], (((1,), (1,)), ((), ())), preferred_element_type=F32)

        @pl.when(j == 0)
        def _():
            acc_ref[...] = part

        @pl.when(j > 0)
        def _():
            acc_ref[...] += part

        @pl.when(j == nchunk - 1)
        def _():
            dh_ref[...] = acc_ref[...]

    return _pcall(body, name=name, grid=(T // tm, nchunk),
                  in_specs=[pl.BlockSpec((tm, dm), lambda i, j: (i, 0)), pl.BlockSpec((tm, 2 * LANE), lambda i, j: (i, j)),
                            pl.BlockSpec((LANE, dm), lambda i, j: (j, 0)), pl.BlockSpec((dm, 2 * LANE), lambda i, j: (0, j))],
                  out_specs=[pl.BlockSpec((tm, 2 * LANE), lambda i, j: (i, j)), pl.BlockSpec((tm, dm), lambda i, j: (i, 0))],
                  out_shape=[_sds(gu.shape, MX), _sds((T, dm), F32)], scratch=[pltpu.VMEM((tm, dm), F32)])(df, gu, w_out, w_gu)


def _rsq(x):
    return lax.rsqrt(jnp.mean(x * x, axis=-1, keepdims=True) + EPS)


def _sigmoid(x):
    return 1.0 / (1.0 + jnp.exp(-x))


def _row_spec(tm, w):
    return pl.BlockSpec((tm, w), lambda i: (i, 0))


def _fwd_block(x_prev, f_prev, mod3, gain, *, sub, coef, name, tm=256):
    T, dm = x_prev.shape
    tps = (T // mod3.shape[0]) // tm
    has_f = f_prev is not None
    mod_spec = pl.BlockSpec((1, 9, dm), lambda i: (i // tps, 0, 0))
    vec_spec = pl.BlockSpec((1, dm), lambda i: (0, 0))

    def body(*refs):
        if has_f:
            x_ref, f_ref, mod_ref, n_ref, xo_ref, h_ref = refs
            x = x_ref[...] + coef * mod_ref[0, 3 * sub - 1:3 * sub, :] * f_ref[...]
            xo_ref[...] = x
        else:
            x_ref, mod_ref, n_ref, h_ref = refs
            x = x_ref[...]
        xn = x * _rsq(x) * n_ref[...]
        h = xn * (1.0 + mod_ref[0, 3 * sub + 1:3 * sub + 2, :]) + mod_ref[0, 3 * sub:3 * sub + 1, :]
        h_ref[...] = h.astype(h_ref.dtype)

    row = _row_spec(tm, dm)
    if has_f:
        return _pcall(body, name=name, grid=(T // tm,), in_specs=[row, row, mod_spec, vec_spec], out_specs=[row, row],
                      out_shape=[_sds((T, dm), F32), _sds((T, dm), MX)])(x_prev, f_prev, mod3, gain)
    return _pcall(body, name=name, grid=(T // tm,), in_specs=[row, mod_spec, vec_spec], out_specs=row,
                  out_shape=_sds((T, dm), MX))(x_prev, mod3, gain)


def _final(x2, f2, tgt, mod3, *, name, tm=256):
    T, dm = x2.shape
    tps = (T // mod3.shape[0]) // tm
    mod_spec = pl.BlockSpec((1, 9, dm), lambda i: (i // tps, 0, 0))
    row = _row_spec(tm, dm)
    stat_spec = pl.BlockSpec((1, SUBLANE, dm), lambda i: (i // tps, 0, 0))
    loss_spec = pl.BlockSpec((SUBLANE, LANE), lambda i: (0, 0))

    def body(x_ref, f_ref, t_ref, mod_ref, dy_ref, df_ref, st_ref, loss_ref):
        i = pl.program_id(0)
        g = mod_ref[0, 8:9, :]
        f = f_ref[...]
        err = x_ref[...] + 0.5 * g * f - t_ref[...]
        dy = err * (1.0 / dm)
        dy_ref[...] = dy
        df_ref[...] = (0.5 * g * dy).astype(df_ref.dtype)
        dgate = jnp.sum(0.5 * dy * f, axis=0, keepdims=True)
        part = 0.5 * jnp.sum(jnp.sum(err * err, axis=0, keepdims=True), axis=1, keepdims=True) * (1.0 / dm)

        @pl.when(i % tps == 0)
        def _():
            st_ref[...] = jnp.zeros_like(st_ref)

        @pl.when(i == 0)
        def _():
            loss_ref[...] = jnp.zeros_like(loss_ref)

        st_ref[0, 0:1, :] += dgate
        loss_ref[...] += jnp.broadcast_to(part, loss_ref.shape)

    return _pcall(body, name=name, grid=(T // tm,), in_specs=[row, row, row, mod_spec],
                  out_specs=[row, row, stat_spec, loss_spec],
                  out_shape=[_sds((T, dm), F32), _sds((T, dm), MX), _sds((mod3.shape[0], SUBLANE, dm), F32),
                             _sds((SUBLANE, LANE), F32)])(x2, f2, tgt, mod3)


def _bwd_block(x_cur, dh, dx_in, f_prev, mod3, gain, *, sub, coef, name, tm=256):
    T, dm = x_cur.shape
    nb = mod3.shape[0]
    tps = (T // nb) // tm
    has_f = f_prev is not None
    mod_spec = pl.BlockSpec((1, 9, dm), lambda i: (i // tps, 0, 0))
    vec_spec = pl.BlockSpec((1, dm), lambda i: (0, 0))
    stat_spec = pl.BlockSpec((1, SUBLANE, dm), lambda i: (i // tps, 0, 0))
    row = _row_spec(tm, dm)

    def body(*refs):
        if has_f:
            x_ref, dh_ref, dxi_ref, f_ref, mod_ref, n_ref, dx_ref, df_ref, st_ref = refs
        else:
            x_ref, dh_ref, dxi_ref, mod_ref, n_ref, dx_ref, st_ref = refs
        i = pl.program_id(0)
        x = x_ref[...]
        r = _rsq(x)
        xhat = x * r
        n = n_ref[...]
        dhv = dh_ref[...]
        d_shift = jnp.sum(dhv, axis=0, keepdims=True)
        d_scale = jnp.sum(dhv * (xhat * n), axis=0, keepdims=True)
        dxn = dhv * (1.0 + mod_ref[0, 3 * sub + 1:3 * sub + 2, :])
        d_gain = jnp.sum(dxn * xhat, axis=0, keepdims=True)
        dxhat = dxn * n
        dx = dxi_ref[...] + r * (dxhat - xhat * jnp.mean(dxhat * xhat, axis=-1, keepdims=True))
        dx_ref[...] = dx

        @pl.when(i % tps == 0)
        def _():
            st_ref[...] = jnp.zeros_like(st_ref)

        st_ref[0, 0:1, :] += d_shift
        st_ref[0, 1:2, :] += d_scale
        st_ref[0, 2:3, :] += d_gain
        if has_f:
            f = f_ref[...]
            st_ref[0, 3:4, :] += jnp.sum(coef * dx * f, axis=0, keepdims=True)
            df_ref[...] = (coef * mod_ref[0, 3 * sub - 1:3 * sub, :] * dx).astype(df_ref.dtype)

    st_shape = _sds((nb, SUBLANE, dm), F32)
    if has_f:
        return _pcall(body, name=name, grid=(T // tm,), in_specs=[row, row, row, row, mod_spec, vec_spec],
                      out_specs=[row, row, stat_spec],
                      out_shape=[_sds((T, dm), F32), _sds((T, dm), MX), st_shape])(x_cur, dh, dx_in, f_prev, mod3, gain)
    return _pcall(body, name=name, grid=(T // tm,), in_specs=[row, row, row, mod_spec, vec_spec],
                  out_specs=[row, stat_spec], out_shape=[_sds((T, dm), F32), st_shape])(x_cur, dh, dx_in, mod3, gain)


def _merge_fwd(z, br_pool, br_mla, *, name, tm=256):
    T = z.shape[0]

    def body(z_ref, bp_ref, bm_ref, o_ref):
        gp = z_ref[:, Z_GP:Z_GP + D]
        gm = z_ref[:, Z_GM:Z_GM + D]
        o_ref[...] = (_sigmoid(gp) * bp_ref[...] + _sigmoid(gm) * bm_ref[...]).astype(o_ref.dtype)

    return _pcall(body, name=name, grid=(T // tm,), in_specs=[_row_spec(tm, Z_W), _row_spec(tm, D), _row_spec(tm, D)],
                  out_specs=_row_spec(tm, D), out_shape=_sds((T, D), MX))(z, br_pool, br_mla)


def _merge_bwd(z, br_pool, br_mla, dmerged, *, name, tm=256):
    T = z.shape[0]

    def body(z_ref, bp_ref, bm_ref, dm_ref, dbp_ref, dbm_ref, dg_ref):
        dm = dm_ref[...]
        sp = _sigmoid(z_ref[:, Z_GP:Z_GP + D])
        sm = _sigmoid(z_ref[:, Z_GM:Z_GM + D])
        dbp_ref[...] = (dm * sp).astype(dbp_ref.dtype)
        dbm_ref[...] = (dm * sm).astype(dbm_ref.dtype)
        dg_ref[:, :D] = (dm * bp_ref[...] * sp * (1.0 - sp)).astype(dg_ref.dtype)
        dg_ref[:, D:] = (dm * bm_ref[...] * sm * (1.0 - sm)).astype(dg_ref.dtype)

    return _pcall(body, name=name, grid=(T // tm,),
                  in_specs=[_row_spec(tm, Z_W), _row_spec(tm, D), _row_spec(tm, D), _row_spec(tm, D)],
                  out_specs=[_row_spec(tm, D), _row_spec(tm, D), _row_spec(tm, 2 * D)],
                  out_shape=[_sds((T, D), MX), _sds((T, D), MX), _sds((T, 2 * D), MX)])(z, br_pool, br_mla, dmerged)


def _shift_down(x, k, row):
    return jnp.where(row >= k, pltpu.roll(x, k, 0), 0.0)


def _shift_up(x, k, row, n):
    return jnp.where(row < n - k, pltpu.roll(x, n - k, 0), 0.0)


def _pool_fwd(z, pool_grp, pool_scale, *, nb, name):
    T = z.shape[0]
    S = T // nb
    blk = pl.BlockSpec((S, LANE), lambda b, g: (b, g))

    def body(u_ref, w_ref, s_ref, pooled_ref, mixed_ref, scaled_ref):
        g = pl.program_id(1)
        u = u_ref[...]
        row = lax.broadcasted_iota(jnp.int32, u.shape, 0)
        s2 = u + _shift_down(u, 1, row)
        s4 = s2 + _shift_down(s2, 2, row)
        s8 = s4 + _shift_down(s4, 4, row)
        s16 = s8 + _shift_down(s8, 8, row)
        win = jnp.where(g == 0, s2, jnp.where(g == 1, s4, jnp.where(g == 2, s8, s16)))
        width = lax.shift_left(jnp.int32(2), g)
        cnt = jnp.minimum(row + 1, width).astype(F32)
        pooled = (win / cnt - u).astype(MX)
        pooled_ref[...] = pooled
        mixed = jnp.dot(pooled, w_ref[0], preferred_element_type=F32)
        mixed_ref[...] = mixed
        scaled_ref[...] = (mixed * s_ref[...]).astype(scaled_ref.dtype)

    return _pcall(body, name=name, grid=(nb, POOL_G),
                  in_specs=[blk, pl.BlockSpec((1, LANE, LANE), lambda b, g: (g, 0, 0)),
                            pl.BlockSpec((1, LANE), lambda b, g: (0, g))],
                  out_specs=[blk, blk, blk],
                  out_shape=[_sds((T, POOL_W), MX), _sds((T, POOL_W), F32), _sds((T, POOL_W), MX)])(z, pool_grp, pool_scale)


def _pool_bwd(dscaled, mixed, pooled, pool_grp, pool_scale, *, nb, name):
    T = dscaled.shape[0]
    S = T // nb
    blk = pl.BlockSpec((S, LANE), lambda g, b: (b, g))

    def body(ds_ref, mixed_ref, pooled_ref, w_ref, s_ref, du_ref, dw_ref, dsc_ref):
        g = pl.program_id(0)
        b = pl.program_id(1)
        ds = ds_ref[...]
        dsc_ref[0] = jnp.sum(ds * mixed_ref[...], axis=0, keepdims=True)
        dmixed = (ds * s_ref[...]).astype(MX)
        dw = lax.dot_general(pooled_ref[...], dmixed, (((0,), (0,)), ((), ())), preferred_element_type=F32)

        @pl.when(b == 0)
        def _():
            dw_ref[0] = dw

        @pl.when(b > 0)
        def _():
            dw_ref[0] += dw
        dpooled = lax.dot_general(dmixed, w_ref[0], (((1,), (1,)), ((), ())), preferred_element_type=F32)
        row = lax.broadcasted_iota(jnp.int32, dpooled.shape, 0)
        width = lax.shift_left(jnp.int32(2), g)
        q = dpooled / jnp.minimum(row + 1, width).astype(F32)
        r2 = q + _shift_up(q, 1, row, S)
        r4 = r2 + _shift_up(r2, 2, row, S)
        r8 = r4 + _shift_up(r4, 4, row, S)
        r16 = r8 + _shift_up(r8, 8, row, S)
        win = jnp.where(g == 0, r2, jnp.where(g == 1, r4, jnp.where(g == 2, r8, r16)))
        du_ref[...] = (win - dpooled).astype(du_ref.dtype)

    return _pcall(body, name=name, grid=(POOL_G, nb),
                  in_specs=[blk, blk, blk, pl.BlockSpec((1, LANE, LANE), lambda g, b: (g, 0, 0)),
                            pl.BlockSpec((1, LANE), lambda g, b: (0, g))],
                  out_specs=[blk, pl.BlockSpec((1, LANE, LANE), lambda g, b: (g, 0, 0)),
                             pl.BlockSpec((1, 1, LANE), lambda g, b: (b, 0, g))],
                  out_shape=[_sds((T, POOL_W), MX), _sds((POOL_G, LANE, LANE), F32), _sds((nb, 1, POOL_W), F32)],
                  )(dscaled, mixed, pooled, pool_grp, pool_scale)


def _lat_fwd(z, q_gain, kv_gain, *, name, tm=256):
    T = z.shape[0]

    def body(z_ref, qg_ref, kg_ref, qn_ref, kvn_ref):
        ql = z_ref[:, Z_QL:Z_QL + QL]
        kv = z_ref[:, Z_KV:Z_KV + KVL]
        qn_ref[...] = (ql * _rsq(ql) * qg_ref[...]).astype(qn_ref.dtype)
        kvn_ref[...] = (kv * _rsq(kv) * kg_ref[...]).astype(kvn_ref.dtype)

    return _pcall(body, name=name, grid=(T // tm,),
                  in_specs=[_row_spec(tm, LAT_W), pl.BlockSpec((1, QL), lambda i: (0, 0)), pl.BlockSpec((1, KVL), lambda i: (0, 0))],
                  out_specs=[_row_spec(tm, QL), _row_spec(tm, KVL)],
                  out_shape=[_sds((T, QL), MX), _sds((T, KVL), MX)])(z, q_gain, kv_gain)


def _lat_bwd(z, dqn, dkvn, q_gain, kv_gain, *, nb, name, tm=256):
    T = z.shape[0]
    tps = (T // nb) // tm

    def norm_bwd(x, dy, gain):
        r = _rsq(x)
        xhat = x * r
        dgain = jnp.sum(dy * xhat, axis=0, keepdims=True)
        dxhat = dy * gain
        return r * (dxhat - xhat * jnp.mean(dxhat * xhat, axis=-1, keepdims=True)), dgain

    def body(z_ref, dq_ref, dkv_ref, qg_ref, kg_ref, dql_ref, dkvl_ref, sq_ref, sk_ref):
        i = pl.program_id(0)
        dql, dqg = norm_bwd(z_ref[:, Z_QL:Z_QL + QL], dq_ref[...], qg_ref[...])
        dkvl, dkg = norm_bwd(z_ref[:, Z_KV:Z_KV + KVL], dkv_ref[...], kg_ref[...])
        dql_ref[...] = dql.astype(dql_ref.dtype)
        dkvl_ref[...] = dkvl.astype(dkvl_ref.dtype)

        @pl.when(i % tps == 0)
        def _():
            sq_ref[...] = jnp.zeros_like(sq_ref)
            sk_ref[...] = jnp.zeros_like(sk_ref)

        sq_ref[0, 0:1, :] += dqg
        sk_ref[0, 0:1, :] += dkg

    return _pcall(body, name=name, grid=(T // tm,),
                  in_specs=[_row_spec(tm, LAT_W), _row_spec(tm, QL), _row_spec(tm, KVL),
                            pl.BlockSpec((1, QL), lambda i: (0, 0)), pl.BlockSpec((1, KVL), lambda i: (0, 0))],
                  out_specs=[_row_spec(tm, QL), _row_spec(tm, KVL),
                             pl.BlockSpec((1, SUBLANE, QL), lambda i: (i // tps, 0, 0)),
                             pl.BlockSpec((1, SUBLANE, KVL), lambda i: (i // tps, 0, 0))],
                  out_shape=[_sds((T, QL), MX), _sds((T, KVL), MX), _sds((nb, SUBLANE, QL), F32), _sds((nb, SUBLANE, KVL), F32)],
                  )(z, dqn, dkvn, q_gain, kv_gain)


def _lane_masks(shape):
    lane = lax.broadcasted_iota(jnp.int32, shape, len(shape) - 1)
    m_n = lane < NOPE
    m_r = jnp.logical_and(lane >= KR_LANE, lane < KR_LANE + ROPE)
    first_half = lane < KR_LANE + ROPE // 2
    return m_n, m_r, first_half


def _rot(y, first_half):
    return jnp.where(first_half, -pltpu.roll(y, LANE - ROPE // 2, 1), pltpu.roll(y, ROPE // 2, 1))


def _rot_t(v, first_half, m_r):
    return jnp.where(m_r, jnp.where(first_half, pltpu.roll(v, LANE - ROPE // 2, 1), -pltpu.roll(v, ROPE // 2, 1)), 0.0)


def _prep_fwd(qp, kvp, z, cos, sin, q_gain, k_gain, *, name, tm=256):
    T = qp.shape[0]
    slab = pl.BlockSpec((tm, LANE), lambda i: (i, 0))
    kr_spec = pl.BlockSpec((tm, LANE), lambda i: (i, Z_KR // LANE))
    vec = pl.BlockSpec((1, LANE), lambda i: (0, 0))

    def body(qp_ref, kvp_ref, kr_ref, cos_ref, sin_ref, qg_ref, kg_ref, q_ref, k_ref, v_ref):
        m_n, m_r, first_half = _lane_masks((tm, LANE))
        c = cos_ref[...]
        s = sin_ref[...]
        qg = qg_ref[...]
        kg = kg_ref[...]
        xr = kr_ref[...]
        rr = lax.rsqrt(jnp.sum(xr * xr, axis=-1, keepdims=True) * (1.0 / ROPE) + EPS)
        yr = xr * rr * kg
        kr = jnp.where(m_r, yr * c + _rot(yr, first_half) * s, 0.0)
        for h in range(NH):
            x = qp_ref[:, h * LANE:(h + 1) * LANE]
            x2 = x * x
            rn = lax.rsqrt(jnp.sum(jnp.where(m_n, x2, 0.0), axis=-1, keepdims=True) * (1.0 / NOPE) + EPS)
            rq = lax.rsqrt(jnp.sum(jnp.where(m_r, x2, 0.0), axis=-1, keepdims=True) * (1.0 / ROPE) + EPS)
            y = x * jnp.where(m_n, rn, jnp.where(m_r, rq, 0.0)) * qg
            q_ref[:, h * LANE:(h + 1) * LANE] = (y * c + _rot(y, first_half) * s).astype(q_ref.dtype)
            xk = kvp_ref[:, h * LANE:(h + 1) * LANE]
            rk = lax.rsqrt(jnp.sum(jnp.where(m_n, xk * xk, 0.0), axis=-1, keepdims=True) * (1.0 / NOPE) + EPS)
            k_ref[:, h * LANE:(h + 1) * LANE] = (jnp.where(m_n, xk * rk * kg, 0.0) + kr).astype(k_ref.dtype)
        v_ref[...] = kvp_ref[:, NH * LANE:].astype(v_ref.dtype)

    return _pcall(body, name=name, grid=(T // tm,),
                  in_specs=[_row_spec(tm, NH * LANE), _row_spec(tm, NH * LANE + NH * NOPE), kr_spec, slab, slab, vec, vec],
                  out_specs=[_row_spec(tm, NH * LANE), _row_spec(tm, NH * LANE), _row_spec(tm, NH * NOPE)],
                  out_shape=[_sds((T, NH * LANE), MX), _sds((T, NH * LANE), MX), _sds((T, NH * NOPE), MX)],
                  )(qp, kvp, z, cos, sin, q_gain, k_gain)


def _prep_bwd(dq, dk, dv, qp, kvp, z, cos, sin, q_gain, k_gain, *, nb, name, tm=256):
    T = qp.shape[0]
    tps = (T // nb) // tm
    slab = pl.BlockSpec((tm, LANE), lambda i: (i, 0))
    kr_spec = pl.BlockSpec((tm, LANE), lambda i: (i, Z_KR // LANE))
    vec = pl.BlockSpec((1, LANE), lambda i: (0, 0))

    def body(dq_ref, dk_ref, dv_ref, qp_ref, kvp_ref, kr_ref, cos_ref, sin_ref, qg_ref, kg_ref,
             dqp_ref, dkvp_ref, dkr_ref, st_ref):
        i = pl.program_id(0)
        m_n, m_r, first_half = _lane_masks((tm, LANE))
        c = cos_ref[...]
        s = sin_ref[...]
        qg = qg_ref[...]
        kg = kg_ref[...]
        dqg = jnp.zeros((1, LANE), F32)
        dkg = jnp.zeros((1, LANE), F32)
        dkr_sum = jnp.zeros((tm, LANE), F32)
        for h in range(NH):
            x = qp_ref[:, h * LANE:(h + 1) * LANE]
            x2 = x * x
            rn = lax.rsqrt(jnp.sum(jnp.where(m_n, x2, 0.0), axis=-1, keepdims=True) * (1.0 / NOPE) + EPS)
            rq = lax.rsqrt(jnp.sum(jnp.where(m_r, x2, 0.0), axis=-1, keepdims=True) * (1.0 / ROPE) + EPS)
            rfac = jnp.where(m_n, rn, jnp.where(m_r, rq, 0.0))
            xhat = x * rfac
            do = dq_ref[:, h * LANE:(h + 1) * LANE]
            dy = do * c + _rot_t(do * s, first_half, m_r)
            dqg = dqg + jnp.sum(dy * xhat, axis=0, keepdims=True)
            dxhat = dy * qg
            t = dxhat * xhat
            mean_n = jnp.sum(jnp.where(m_n, t, 0.0), axis=-1, keepdims=True) * (1.0 / NOPE)
            mean_r = jnp.sum(jnp.where(m_r, t, 0.0), axis=-1, keepdims=True) * (1.0 / ROPE)
            dqp_ref[:, h * LANE:(h + 1) * LANE] = (
                rfac * (dxhat - xhat * jnp.where(m_n, mean_n, jnp.where(m_r, mean_r, 0.0)))).astype(dqp_ref.dtype)

            xk = kvp_ref[:, h * LANE:(h + 1) * LANE]
            rk = lax.rsqrt(jnp.sum(jnp.where(m_n, xk * xk, 0.0), axis=-1, keepdims=True) * (1.0 / NOPE) + EPS)
            khat = jnp.where(m_n, xk * rk, 0.0)
            dko = dk_ref[:, h * LANE:(h + 1) * LANE]
            dkn = jnp.where(m_n, dko, 0.0)
            dkg = dkg + jnp.sum(dkn * khat, axis=0, keepdims=True)
            dkhat = dkn * kg
            mean_k = jnp.sum(dkhat * khat, axis=-1, keepdims=True) * (1.0 / NOPE)
            dkvp_ref[:, h * LANE:(h + 1) * LANE] = jnp.where(m_n, rk * (dkhat - khat * mean_k), 0.0).astype(dkvp_ref.dtype)
            dkr_sum = dkr_sum + jnp.where(m_r, dko, 0.0)
        dkvp_ref[:, NH * LANE:] = dv_ref[...].astype(dkvp_ref.dtype)

        xr = kr_ref[...]
        rr = lax.rsqrt(jnp.sum(xr * xr, axis=-1, keepdims=True) * (1.0 / ROPE) + EPS)
        rhat = xr * rr
        dyr = dkr_sum * c + _rot_t(dkr_sum * s, first_half, m_r)
        dkg = dkg + jnp.sum(dyr * rhat, axis=0, keepdims=True)
        drhat = dyr * kg
        mean_kr = jnp.sum(drhat * rhat, axis=-1, keepdims=True) * (1.0 / ROPE)
        dkr_ref[...] = jnp.where(m_r, rr * (drhat - rhat * mean_kr), 0.0).astype(dkr_ref.dtype)

        @pl.when(i % tps == 0)
        def _():
            st_ref[...] = jnp.zeros_like(st_ref)

        st_ref[0, 0:1, :] += dqg
        st_ref[0, 1:2, :] += dkg

    return _pcall(body, name=name, grid=(T // tm,),
                  in_specs=[_row_spec(tm, NH * LANE), _row_spec(tm, NH * LANE), _row_spec(tm, NH * NOPE),
                            _row_spec(tm, NH * LANE), _row_spec(tm, NH * LANE + NH * NOPE), kr_spec, slab, slab, vec, vec],
                  out_specs=[_row_spec(tm, NH * LANE), _row_spec(tm, NH * LANE + NH * NOPE), slab,
                             pl.BlockSpec((1, SUBLANE, LANE), lambda i: (i // tps, 0, 0))],
                  out_shape=[_sds((T, NH * LANE), MX), _sds((T, NH * LANE + NH * NOPE), MX), _sds((T, LANE), MX),
                             _sds((nb, SUBLANE, LANE), F32)],
                  )(dq, dk, dv, qp, kvp, z, cos, sin, q_gain, k_gain)


def _causal_mask(qi, kj, tq, tk):
    r = lax.broadcasted_iota(jnp.int32, (tq, tk), 0) + qi * tq
    c = lax.broadcasted_iota(jnp.int32, (tq, tk), 1) + kj * tk
    return c <= r


def _attn_fwd(q, k, v, *, nb, name, tq=256):
    T = q.shape[0]
    S = T // nb
    nq = S // tq
    tk = tq
    npair = NH // 2

    def body(q_ref, k_ref, v_ref, o_ref, lse_ref):
        qi = pl.program_id(2)
        lane = lax.broadcasted_iota(jnp.int32, (tq, LANE), 1)
        outs, lses = [], []
        for hh in range(2):
            qh = q_ref[:, hh * LANE:(hh + 1) * LANE]

            def step(j, carry, hh=hh, qh=qh):
                m, l, acc = carry
                kb = k_ref[pl.ds(pl.multiple_of(j * tk, tk), tk), hh * LANE:(hh + 1) * LANE]
                vb = v_ref[pl.ds(pl.multiple_of(j * tk, tk), tk), :]
                s = lax.dot_general(qh, kb, (((1,), (1,)), ((), ())), preferred_element_type=F32) * ATTN_SCALE
                s = jnp.where(_causal_mask(qi, j, tq, tk), s, NEG)
                m_new = jnp.maximum(m, jnp.max(s, axis=-1, keepdims=True))
                p = jnp.exp(s - m_new)
                alpha = jnp.exp(m - m_new)
                l = alpha * l + jnp.sum(p, axis=-1, keepdims=True)
                acc = alpha * acc + jnp.dot(p.astype(MX), vb, preferred_element_type=F32)
                return m_new, l, acc

            m, l, acc = lax.fori_loop(0, qi + 1, step, (jnp.full((tq, 1), NEG, F32), jnp.zeros((tq, 1), F32),
                                                        jnp.zeros((tq, LANE), F32)))
            outs.append(acc / l)
            lses.append(m + jnp.log(l))
        o_ref[...] = jnp.where(lane < NOPE, outs[0], outs[1]).astype(o_ref.dtype)
        lse_ref[...] = jnp.where(lane < NOPE, lses[0], lses[1])

    return _pcall(body, name=name, grid=(nb, npair, nq),
                  in_specs=[pl.BlockSpec((tq, 2 * LANE), lambda b, p, i: (b * nq + i, p)),
                            pl.BlockSpec((S, 2 * LANE), lambda b, p, i: (b, p)),
                            pl.BlockSpec((S, LANE), lambda b, p, i: (b, p))],
                  out_specs=[pl.BlockSpec((tq, LANE), lambda b, p, i: (b * nq + i, p)),
                             pl.BlockSpec((tq, LANE), lambda b, p, i: (b * nq + i, p))],
                  out_shape=[_sds((T, NH * NOPE), MX), _sds((T, NH * NOPE), F32)])(q, k, v)


def _attn_bwd(q, k, v, o, lse, do, *, nb, name, tq=256):
    T = q.shape[0]
    S = T // nb
    nq = S // tq
    tk = tq
    npair = NH // 2

    def body(q_ref, k_ref, v_ref, o_ref, lse_ref, do_ref, dq_ref, dk_ref, dv_ref):
        dk_ref[...] = jnp.zeros_like(dk_ref)
        dv_ref[...] = jnp.zeros_like(dv_ref)
        lane = lax.broadcasted_iota(jnp.int32, (tq, LANE), 1)
        for hh in range(2):
            mine = jnp.logical_and(lane >= hh * NOPE, lane < (hh + 1) * NOPE)

            def q_step(qi, _, hh=hh, mine=mine):
                q0 = pl.multiple_of(qi * tq, tq)
                qh = q_ref[pl.ds(q0, tq), hh * LANE:(hh + 1) * LANE]
                dof = jnp.where(mine, do_ref[pl.ds(q0, tq), :], 0.0)
                dob = dof.astype(MX)
                delta = jnp.sum(dof * o_ref[pl.ds(q0, tq), :].astype(F32), axis=-1, keepdims=True)
                lse_h = jnp.sum(jnp.where(lane == hh * NOPE, lse_ref[pl.ds(q0, tq), :], 0.0), axis=-1, keepdims=True)

                def k_step(kj, dq_acc):
                    k0 = pl.multiple_of(kj * tk, tk)
                    kb = k_ref[pl.ds(k0, tk), hh * LANE:(hh + 1) * LANE]
                    vb = v_ref[pl.ds(k0, tk), :]
                    s = lax.dot_general(qh, kb, (((1,), (1,)), ((), ())), preferred_element_type=F32) * ATTN_SCALE
                    p = jnp.where(_causal_mask(qi, kj, tq, tk), jnp.exp(s - lse_h), 0.0)
                    dp = lax.dot_general(dob, vb, (((1,), (1,)), ((), ())), preferred_element_type=F32)
                    ds = (p * (dp - delta) * ATTN_SCALE).astype(MX)
                    pb = p.astype(MX)
                    dk_ref[pl.ds(k0, tk), hh * LANE:(hh + 1) * LANE] += lax.dot_general(
                        ds, qh, (((0,), (0,)), ((), ())), preferred_element_type=F32)
                    dv_ref[pl.ds(k0, tk), :] += lax.dot_general(pb, dob, (((0,), (0,)), ((), ())), preferred_element_type=F32)
                    return dq_acc + jnp.dot(ds, kb, preferred_element_type=F32)

                dq_ref[pl.ds(q0, tq), hh * LANE:(hh + 1) * LANE] = lax.fori_loop(0, qi + 1, k_step, jnp.zeros((tq, LANE), F32))
                return 0

            lax.fori_loop(0, nq, q_step, 0)

    pair256 = pl.BlockSpec((S, 2 * LANE), lambda b, p: (b, p))
    pair128 = pl.BlockSpec((S, LANE), lambda b, p: (b, p))
    return _pcall(body, name=name, grid=(nb, npair),
                  in_specs=[pair256, pair256, pair128, pair128, pair128, pair128],
                  out_specs=[pair256, pair256, pair128],
                  out_shape=[_sds((T, NH * LANE), F32), _sds((T, NH * LANE), F32), _sds((T, NH * NOPE), F32)],
                  )(q, k, v, o, lse, do)


def _layer_fwd_bwd(x, tgt, mod3, cos, sin, W, P):
    nb = mod3.shape[0]
    h1 = _fwd_block(x, None, mod3, P["norm_ffn1"], sub=0, coef=0.0, name="fwd_norm1")
    gu1, a1 = _ffn_up(h1, W["ffn1_in"], name="ffn1_up")
    f1 = _mm(a1, W["ffn1_out"], mode="nn", out_dtype=F32, name="ffn1_out", tk=DFF)
    x1, h2 = _fwd_block(x, f1, mod3, P["norm_mix"], sub=1, coef=0.5, name="fwd_norm2")
    z = _mm(h2, W["w_in"], mode="nn", out_dtype=F32, name="mix_in", tn=1664)
    pooled, mixed, scaled = _pool_fwd(z, W["pool_grp"], P["pool_scale"], nb=nb, name="pool_fwd")
    br_pool = _mm(scaled, W["pool_proj"], mode="nn", out_dtype=F32, name="pool_proj")
    qn, kvn = _lat_fwd(z, P["q_a_norm"], P["kv_a_norm"], name="lat_fwd")
    qp = _mm(qn, W["q_up"], mode="nn", out_dtype=F32, name="q_up")
    kvp = _mm(kvn, W["kv_up"], mode="nn", out_dtype=F32, name="kv_up")
    q, k, v = _prep_fwd(qp, kvp, z, cos, sin, P["q_gain"], P["k_gain"], name="prep_fwd")
    attn, lse = _attn_fwd(q, k, v, nb=nb, name="attn_fwd")
    br_mla = _mm(attn, W["mla_proj"], mode="nn", out_dtype=F32, name="mla_proj")
    merged = _merge_fwd(z, br_pool, br_mla, name="merge_fwd")
    mo = _mm(merged, W["w_out"], mode="nn", out_dtype=F32, name="mix_out")
    x2, h3 = _fwd_block(x1, mo, mod3, P["norm_ffn2"], sub=2, coef=1.0, name="fwd_norm3")
    gu2, a2 = _ffn_up(h3, W["ffn2_in"], name="ffn2_up")
    f2 = _mm(a2, W["ffn2_out"], mode="nn", out_dtype=F32, name="ffn2_out", tk=DFF)
    dy, df2, st_fin, loss = _final(x2, f2, tgt, mod3, name="loss_head")

    G = {}
    tokens = x.shape[0]
    G["ffn2_out"] = _mm(a2, df2, mode="tn", out_dtype=F32, name="d_ffn2_out", tm=256, tn=D, tk=tokens, n_outer=True)
    dgu2, dh3 = _ffn_bwd(df2, gu2, W["ffn2_out"], W["ffn2_in"], name="ffn2_bwd")
    G["ffn2_in"] = _mm(h3, dgu2, mode="tn", out_dtype=F32, name="d_ffn2_in", tm=256, tn=1408, tk=tokens, n_outer=True)
    dx2, dmo, st3 = _bwd_block(x2, dh3, dy, mo, mod3, P["norm_ffn2"], sub=2, coef=1.0, name="bwd_norm3")
    G["w_out"] = _mm(merged, dmo, mode="tn", out_dtype=F32, name="d_mix_out")
    dmerged = _mm(dmo, W["w_out"], mode="nt", out_dtype=F32, name="d_merged")
    dbr_pool, dbr_mla, dgates = _merge_bwd(z, br_pool, br_mla, dmerged, name="merge_bwd")
    G["pool_proj"] = _mm(scaled, dbr_pool, mode="tn", out_dtype=F32, name="d_pool_proj")
    dscaled = _mm(dbr_pool, W["pool_proj"], mode="nt", out_dtype=F32, name="d_pool_scaled")
    du_pool, d_pool_grp, d_pool_scale = _pool_bwd(dscaled, mixed, pooled, W["pool_grp"], P["pool_scale"], nb=nb, name="pool_bwd")
    G["mla_proj"] = _mm(attn, dbr_mla, mode="tn", out_dtype=F32, name="d_mla_proj")
    dattn = _mm(dbr_mla, W["mla_proj"], mode="nt", out_dtype=F32, name="d_attn")
    dq, dk, dv = _attn_bwd(q, k, v, attn, lse, dattn, nb=nb, name="attn_bwd")
    dqp, dkvp, dkr, st_prep = _prep_bwd(dq, dk, dv, qp, kvp, z, cos, sin, P["q_gain"], P["k_gain"], nb=nb, name="prep_bwd")
    G["q_up"] = _mm(qn, dqp, mode="tn", out_dtype=F32, name="d_q_up")
    dqn = _mm(dqp, W["q_up"], mode="nt", out_dtype=F32, name="d_qn")
    G["kv_up"] = _mm(kvn, dkvp, mode="tn", out_dtype=F32, name="d_kv_up")
    dkvn = _mm(dkvp, W["kv_up"], mode="nt", out_dtype=F32, name="d_kvn", tk=1536)
    dql, dkvl, st_q, st_kv = _lat_bwd(z, dqn, dkvn, P["q_a_norm"], P["kv_a_norm"], nb=nb, name="lat_bwd")
    dz = jnp.concatenate([du_pool, dql, dkvl, dkr, dgates], axis=1)
    G["w_in"] = _mm(h2, dz, mode="tn", out_dtype=F32, name="d_mix_in", tn=1664)
    dh2 = _mm(dz, W["w_in"], mode="nt", out_dtype=F32, name="d_mix_h", tk=1664)
    dx1, df1, st2 = _bwd_block(x1, dh2, dx2, f1, mod3, P["norm_mix"], sub=1, coef=0.5, name="bwd_norm2")
    G["ffn1_out"] = _mm(a1, df1, mode="tn", out_dtype=F32, name="d_ffn1_out", tm=256, tn=D, tk=tokens, n_outer=True)
    dgu1, dh1 = _ffn_bwd(df1, gu1, W["ffn1_out"], W["ffn1_in"], name="ffn1_bwd")
    G["ffn1_in"] = _mm(h1, dgu1, mode="tn", out_dtype=F32, name="d_ffn1_in", tm=256, tn=1408, tk=tokens, n_outer=True)
    grad_x, st1 = _bwd_block(x, dh1, dx1, None, mod3, P["norm_ffn1"], sub=0, coef=0.0, name="bwd_norm1")

    dmod = jnp.stack([st1[:, 0], st1[:, 1], st2[:, 3], st2[:, 0], st2[:, 1], st3[:, 3], st3[:, 0], st3[:, 1], st_fin[:, 0]],
                     axis=1)
    small = {
        "norm_ffn1": st1[:, 2], "norm_mix": st2[:, 2], "norm_ffn2": st3[:, 2],
        "pool_grp": d_pool_grp, "pool_scale": d_pool_scale[:, 0],
        "q_a_norm": st_q[:, 0], "kv_a_norm": st_kv[:, 0],
        "q_gain": st_prep[:, 0], "k_gain": st_prep[:, 1],
    }
    return loss, grad_x, G, small, dmod


def _w_in_to_kernel(w):
    k = w.shape[0]
    zeros = lambda n: jnp.zeros((k, n), w.dtype)
    return jnp.concatenate([w[:, 0:1152], zeros(KR_LANE), w[:, 1152:1184], zeros(LANE - KR_LANE - ROPE), w[:, 1184:]], axis=1)


def _w_in_from_kernel(g):
    return jnp.concatenate([g[:, 0:1152], g[:, Z_KR + KR_LANE:Z_KR + KR_LANE + ROPE], g[:, Z_GP:]], axis=1)


def _q_up_to_kernel(w):
    k = w.shape[0]
    return jnp.pad(w.reshape(k, NH, NOPE + ROPE), ((0, 0), (0, 0), (0, LANE - NOPE - ROPE))).reshape(k, NH * LANE)


def _q_up_from_kernel(g):
    k = g.shape[0]
    return g.reshape(k, NH, LANE)[:, :, :NOPE + ROPE].reshape(k, NH * (NOPE + ROPE))


def _kv_up_to_kernel(w):
    k = w.shape[0]
    w3 = w.reshape(k, NH, 2 * NOPE)
    kpart = jnp.pad(w3[:, :, :NOPE], ((0, 0), (0, 0), (0, LANE - NOPE))).reshape(k, NH * LANE)
    return jnp.concatenate([kpart, w3[:, :, NOPE:].reshape(k, NH * NOPE)], axis=1)


def _kv_up_from_kernel(g):
    k = g.shape[0]
    kpart = g[:, :NH * LANE].reshape(k, NH, LANE)[:, :, :NOPE]
    vpart = g[:, NH * LANE:].reshape(k, NH, NOPE)
    return jnp.concatenate([kpart, vpart], axis=2).reshape(k, NH * 2 * NOPE)


def _gain_slab(nope, rope):
    return jnp.concatenate([nope, rope, jnp.zeros((1, LANE - NOPE - ROPE), nope.dtype)], axis=1)


def _rope_tables(positions):
    inv_freq = 10000.0 ** (-jnp.arange(0, ROPE, 2, dtype=F32) / ROPE)
    ang = positions.astype(F32)[:, None] * inv_freq
    ang = jnp.concatenate([ang, ang], axis=-1)
    t = positions.shape[0]
    cos = jnp.concatenate([jnp.ones((t, KR_LANE), F32), jnp.cos(ang), jnp.ones((t, LANE - KR_LANE - ROPE), F32)], axis=1)
    sin = jnp.concatenate([jnp.zeros((t, KR_LANE), F32), jnp.sin(ang), jnp.zeros((t, LANE - KR_LANE - ROPE), F32)], axis=1)
    return cos, sin


def _coords():
    return lax.axis_index("x"), lax.axis_index("y"), lax.axis_index("c")


HBM_SPEC = pl.BlockSpec(memory_space=pl.ANY)
VMEM_SPEC = pl.BlockSpec(memory_space=pltpu.VMEM)


def _gather8(v, *, name):
    r, c = v.shape

    def body(v_ref, out_ref, send_sems, recv_sems, local_sem):
        x, y, cc = _coords()
        me = 4 * x + 2 * y + cc
        mine = pltpu.make_async_copy(v_ref, out_ref.at[me], local_sem)
        mine.start()
        copies = []
        for kk in range(1, N_DEV):
            peer = (x ^ (kk >> 2), y ^ ((kk >> 1) & 1), cc ^ (kk & 1))
            cp = pltpu.make_async_remote_copy(src_ref=v_ref, dst_ref=out_ref.at[me], send_sem=send_sems.at[kk - 1],
                                              recv_sem=recv_sems.at[kk - 1], device_id=peer, device_id_type=MESH)
            cp.start()
            copies.append(cp)
        for kk in range(1, N_DEV):
            peer_slot = me ^ kk
            pltpu.make_async_remote_copy(src_ref=v_ref, dst_ref=out_ref.at[peer_slot], send_sem=send_sems.at[kk - 1],
                                         recv_sem=recv_sems.at[kk - 1], device_id=(x, y, cc), device_id_type=MESH).wait_recv()
        for cp in copies:
            cp.wait_send()
        mine.wait()

    return _pcall(body, name=name, out_shape=_sds((N_DEV, r, c), v.dtype), in_specs=[VMEM_SPEC], out_specs=VMEM_SPEC,
                  scratch=[pltpu.SemaphoreType.DMA((N_DEV - 1,)), pltpu.SemaphoreType.DMA((N_DEV - 1,)), pltpu.SemaphoreType.DMA])(v)


CHIP_RELS = ((1, 0), (0, 1), (1, 1))


def _gather_weights(shards, *, name):
    n = len(shards)

    def body(*refs):
        ins, outs = refs[:n], refs[n:2 * n]
        isend, irecv, dsend, drecv, lsem = refs[2 * n:]
        x, y, cc = _coords()
        chip = 2 * x + y
        sib = (x, y, 1 - cc)
        local_copies, sends = [], []
        for a in range(n):
            half = ins[a].shape[0] // 2
            mine = pl.ds(cc * half, half)
            loc = pltpu.make_async_copy(ins[a], outs[a].at[chip], lsem.at[a])
            loc.start()
            local_copies.append(loc)
            for j, (dx, dy) in enumerate(CHIP_RELS):
                cp = pltpu.make_async_remote_copy(src_ref=ins[a].at[mine], dst_ref=outs[a].at[chip, mine],
                                                  send_sem=isend.at[a, j], recv_sem=irecv.at[a, j],
                                                  device_id=(x ^ dx, y ^ dy, cc), device_id_type=MESH)
                cp.start()
                sends.append(cp)
        for a in range(n):
            half = ins[a].shape[0] // 2
            mine = pl.ds(cc * half, half)
            for j, (dx, dy) in enumerate(CHIP_RELS):
                src_chip = 2 * (x ^ dx) + (y ^ dy)
                landed = outs[a].at[src_chip, mine]
                pltpu.make_async_remote_copy(src_ref=landed, dst_ref=landed, send_sem=isend.at[a, j], recv_sem=irecv.at[a, j],
                                             device_id=sib, device_id_type=MESH).wait_recv()
                fwd = pltpu.make_async_remote_copy(src_ref=landed, dst_ref=landed, send_sem=dsend.at[a, j],
                                                   recv_sem=drecv.at[a, j], device_id=sib, device_id_type=MESH)
                fwd.start()
                sends.append(fwd)
        for a in range(n):
            half = ins[a].shape[0] // 2
            theirs = pl.ds((1 - cc) * half, half)
            for j, (dx, dy) in enumerate(CHIP_RELS):
                src_chip = 2 * (x ^ dx) + (y ^ dy)
                landed = outs[a].at[src_chip, theirs]
                pltpu.make_async_remote_copy(src_ref=landed, dst_ref=landed, send_sem=dsend.at[a, j], recv_sem=drecv.at[a, j],
                                             device_id=sib, device_id_type=MESH).wait_recv()
        for cp in sends:
            cp.wait_send()
        for loc in local_copies:
            loc.wait()

    return _pcall(body, name=name, out_shape=[_sds((N_CHIP,) + s.shape, s.dtype) for s in shards],
                  in_specs=[HBM_SPEC] * n, out_specs=[HBM_SPEC] * n,
                  scratch=[pltpu.SemaphoreType.DMA((n, 3)), pltpu.SemaphoreType.DMA((n, 3)), pltpu.SemaphoreType.DMA((n, 3)),
                           pltpu.SemaphoreType.DMA((n, 3)), pltpu.SemaphoreType.DMA((n,))])(*shards)


def _swap_halves(parts, *, name):
    n = len(parts)

    def body(*refs):
        ins, outs = refs[:n], refs[n:2 * n]
        ssem, rsem = refs[2 * n:]
        x, y, cc = _coords()
        sib = (x, y, 1 - cc)
        cps = []
        for a in range(n):
            half = ins[a].shape[1] // 2
            cp = pltpu.make_async_remote_copy(src_ref=ins[a].at[:, pl.ds((1 - cc) * half, half)], dst_ref=outs[a],
                                              send_sem=ssem.at[a], recv_sem=rsem.at[a], device_id=sib, device_id_type=MESH)
            cp.start()
            cps.append(cp)
        for cp in cps:
            cp.wait()

    return _pcall(body, name=name,
                  out_shape=[_sds((p.shape[0], p.shape[1] // 2, p.shape[2]), p.dtype) for p in parts],
                  in_specs=[HBM_SPEC] * n, out_specs=[HBM_SPEC] * n,
                  scratch=[pltpu.SemaphoreType.DMA((n,)), pltpu.SemaphoreType.DMA((n,))])(*parts)


def _add_half(full, other, cidx, *, name):
    nch, r, c = full.shape
    half = r // 2
    tr = _pick_rows(half)
    nbk = half // tr
    grid_spec = pltpu.PrefetchScalarGridSpec(
        num_scalar_prefetch=1, grid=(nch, nbk),
        in_specs=[pl.BlockSpec((1, tr, c), lambda j, i, cref: (j, cref[0] * nbk + i, 0)),
                  pl.BlockSpec((1, tr, c), lambda j, i, cref: (j, i, 0))],
        out_specs=pl.BlockSpec((1, tr, c), lambda j, i, cref: (j, i, 0)))

    def body(cref, a_ref, b_ref, o_ref):
        o_ref[...] = (a_ref[...] + b_ref[...]).astype(o_ref.dtype)

    return _pcall(body, name=name, out_shape=_sds((nch, half, c), MX), grid_spec=grid_spec)(cidx, full, other)


def _pick_rows(rows, target=512):
    best = None
    for t in range(16, min(rows, target) + 1, 16):
        if rows % t == 0:
            best = t
    return rows if best is None else best


def _exchange_chips(parts, *, name):
    n = len(parts)

    def body(*refs):
        ins, outs = refs[:n], refs[n:2 * n]
        ssem, rsem, lsem = refs[2 * n:]
        x, y, cc = _coords()
        chip = 2 * x + y
        cps = []
        for a in range(n):
            loc = pltpu.make_async_copy(ins[a].at[chip], outs[a].at[chip], lsem.at[a])
            loc.start()
            cps.append((loc, None))
            for j, (dx, dy) in enumerate(CHIP_RELS):
                peer_chip = 2 * (x ^ dx) + (y ^ dy)
                cp = pltpu.make_async_remote_copy(src_ref=ins[a].at[peer_chip], dst_ref=outs[a].at[chip],
                                                  send_sem=ssem.at[a, j], recv_sem=rsem.at[a, j],
                                                  device_id=(x ^ dx, y ^ dy, cc), device_id_type=MESH)
                cp.start()
                cps.append((cp, peer_chip))
        for a in range(n):
            for j, (dx, dy) in enumerate(CHIP_RELS):
                peer_chip = 2 * (x ^ dx) + (y ^ dy)
                pltpu.make_async_remote_copy(src_ref=ins[a].at[peer_chip], dst_ref=outs[a].at[peer_chip], send_sem=ssem.at[a, j],
                                             recv_sem=rsem.at[a, j], device_id=(x, y, cc), device_id_type=MESH).wait_recv()
        for cp, peer_chip in cps:
            if peer_chip is None:
                cp.wait()
            else:
                cp.wait_send()

    return _pcall(body, name=name, out_shape=[_sds(p.shape, p.dtype) for p in parts],
                  in_specs=[HBM_SPEC] * n, out_specs=[HBM_SPEC] * n,
                  scratch=[pltpu.SemaphoreType.DMA((n, 3)), pltpu.SemaphoreType.DMA((n, 3)), pltpu.SemaphoreType.DMA((n,))])(*parts)


def _sum_chips(q, cidx, *, name):
    nch, h, c = q.shape
    tr = _pick_rows(h)
    nbk = h // tr
    grid_spec = pltpu.PrefetchScalarGridSpec(
        num_scalar_prefetch=1, grid=(nbk,),
        in_specs=[pl.BlockSpec((nch, tr, c), lambda i, cref: (0, i, 0))],
        out_specs=pl.BlockSpec((tr, c), lambda i, cref: (cref[0] * nbk + i, 0)))

    def body(cref, q_ref, o_ref):
        acc = q_ref[0].astype(F32) + q_ref[1].astype(F32)
        acc = acc + q_ref[2].astype(F32)
        o_ref[...] = acc + q_ref[3].astype(F32)

    return _pcall(body, name=name, out_shape=_sds((2 * h, c), F32), grid_spec=grid_spec)(cidx, q)


def _join_halves(fulls, *, name):
    n = len(fulls)

    def body(*refs):
        outs = refs[n:2 * n]
        ssem, rsem = refs[2 * n:]
        x, y, cc = _coords()
        sib = (x, y, 1 - cc)
        cps = []
        for a in range(n):
            h = outs[a].shape[0] // 2
            mine = outs[a].at[pl.ds(cc * h, h)]
            cp = pltpu.make_async_remote_copy(src_ref=mine, dst_ref=mine, send_sem=ssem.at[a], recv_sem=rsem.at[a],
                                              device_id=sib, device_id_type=MESH)
            cp.start()
            cps.append(cp)
        for a in range(n):
            h = outs[a].shape[0] // 2
            theirs = outs[a].at[pl.ds((1 - cc) * h, h)]
            pltpu.make_async_remote_copy(src_ref=theirs, dst_ref=theirs, send_sem=ssem.at[a], recv_sem=rsem.at[a],
                                         device_id=sib, device_id_type=MESH).wait_recv()
        for cp in cps:
            cp.wait_send()

    return _pcall(body, name=name, out_shape=[_sds(p.shape, p.dtype) for p in fulls],
                  in_specs=[HBM_SPEC] * n, out_specs=[HBM_SPEC] * n, aliases={a: a for a in range(n)},
                  scratch=[pltpu.SemaphoreType.DMA((n,)), pltpu.SemaphoreType.DMA((n,))])(*fulls)


def _ada_fwd(c_all, w, b, *, name):
    m, kdim = c_all.shape
    n = w.shape[1]
    tn = _pick(n, 1152)

    def body(c_ref, w_ref, b_ref, o_ref):
        cv = c_ref[...]
        act = (cv * _sigmoid(cv)).astype(MX)
        o_ref[...] = jnp.dot(act, w_ref[...].astype(MX), preferred_element_type=F32) + b_ref[...]

    return _pcall(body, name=name, out_shape=_sds((m, n), F32), grid=(n // tn,),
                  in_specs=[pl.BlockSpec((m, kdim), lambda j: (0, 0)), pl.BlockSpec((kdim, tn), lambda j: (0, j)),
                            pl.BlockSpec((1, tn), lambda j: (0, j))],
                  out_specs=pl.BlockSpec((m, tn), lambda j: (0, j)))(c_all, w, b)


def _ada_bwd(c_all, dmod_cols, *, name):
    m, kdim = c_all.shape
    n = dmod_cols.shape[1]
    tn = _pick(n, 1152)

    def body(c_ref, d_ref, o_ref):
        cv = c_ref[...]
        act = (cv * _sigmoid(cv)).astype(MX)
        o_ref[...] = lax.dot_general(act, d_ref[...].astype(MX), (((0,), (0,)), ((), ())), preferred_element_type=F32)

    return _pcall(body, name=name, out_shape=_sds((kdim, n), F32), grid=(n // tn,),
                  in_specs=[pl.BlockSpec((m, kdim), lambda j: (0, 0)), pl.BlockSpec((m, tn), lambda j: (0, j))],
                  out_specs=pl.BlockSpec((kdim, tn), lambda j: (0, j)))(c_all, dmod_cols)


def _sum_small(v, nseq, rows, *, name):
    n, r, c = v.shape
    extra = r - nseq * rows

    def body(v_ref, o_ref):
        acc = None
        for d in range(n):
            for s in range(nseq):
                t = v_ref[d, s * rows:(s + 1) * rows, :]
                acc = t if acc is None else acc + t
        o_ref[0:rows, :] = acc
        acc = v_ref[0, nseq * rows:, :]
        for d in range(1, n):
            acc = acc + v_ref[d, nseq * rows:, :]
        o_ref[rows:, :] = acc

    return _pcall(body, name=name, out_shape=_sds((rows + extra, c), F32), in_specs=[VMEM_SPEC], out_specs=VMEM_SPEC)(v)


def _adamw(w, g, m, v, *, name):
    r, c = w.shape
    tr = _pick_rows(r, 256)
    spec = pl.BlockSpec((tr, c), lambda i: (i, 0))
    bc1 = 1.0 / (1.0 - ADAM_B1 ** ADAM_STEP)
    bc2 = 1.0 / (1.0 - ADAM_B2 ** ADAM_STEP)

    def body(w_ref, g_ref, m_ref, v_ref, d_ref, mo_ref, vo_ref):
        gv = g_ref[...]
        mn = ADAM_B1 * m_ref[...] + (1.0 - ADAM_B1) * gv
        vn = ADAM_B2 * v_ref[...] + (1.0 - ADAM_B2) * (gv * gv)
        mo_ref[...] = mn
        vo_ref[...] = vn
        d_ref[...] = -ADAM_LR * ((mn * bc1) / (jnp.sqrt(vn * bc2) + ADAM_EPS) + ADAM_WD * w_ref[...])

    out = _sds((r, c), F32)
    return _pcall(body, name=name, out_shape=[out, out, out], grid=(r // tr,), in_specs=[spec] * 4, out_specs=[spec] * 3)(w, g, m, v)


BIG = ("w_ffn1_in", "w_ffn1_out", "w_in", "w_pool_proj", "w_q_up", "w_kv_up", "w_mla_proj", "w_out", "w_ffn2_in", "w_ffn2_out")
ROW_SHARDED = ("w_ffn1_out", "w_out", "w_ffn2_out")
KERNEL_NAME = {"w_ffn1_in": "ffn1_in", "w_ffn1_out": "ffn1_out", "w_in": "w_in", "w_pool_proj": "pool_proj", "w_q_up": "q_up",
               "w_kv_up": "kv_up", "w_mla_proj": "mla_proj", "w_out": "w_out", "w_ffn2_in": "ffn2_in", "w_ffn2_out": "ffn2_out"}
WEIGHTS = ("w_ada", "b_ada", "norm_ffn1", "w_ffn1_in", "w_ffn1_out", "norm_mix", "w_in", "pool_grp", "pool_scale", "w_pool_proj",
           "q_a_norm", "w_q_up", "kv_a_norm", "w_kv_up", "q_norm_nope", "q_norm_rope", "k_norm_nope", "k_norm_rope", "w_mla_proj",
           "w_out", "norm_ffn2", "w_ffn2_in", "w_ffn2_out")
SMALL = ("b_ada", "norm_ffn1", "norm_mix", "pool_grp", "pool_scale", "q_a_norm", "kv_a_norm", "q_norm_nope", "q_norm_rope",
         "k_norm_nope", "k_norm_rope", "norm_ffn2")
SMALL_SEQ = tuple(n for n in SMALL if n != "pool_grp")
SLAB_W = 1024


def _assemble(name, stacked):
    if name in ROW_SHARDED:
        return stacked.reshape(stacked.shape[0] * stacked.shape[1], stacked.shape[2])
    return jnp.transpose(stacked, (1, 0, 2)).reshape(stacked.shape[1], stacked.shape[0] * stacked.shape[2])


def _split(name, full):
    if name in ROW_SHARDED:
        return full.reshape(N_CHIP, full.shape[0] // N_CHIP, full.shape[1])
    return jnp.transpose(full.reshape(full.shape[0], N_CHIP, full.shape[1] // N_CHIP), (1, 0, 2))


GU = ("w_ffn1_in", "w_ffn2_in")
GU_BLK = DFF // 2 // LANE


def _gu_interleave(w):
    k = w.shape[0]
    return jnp.transpose(w.reshape(k, 2, DFF // LANE, LANE), (0, 2, 1, 3)).reshape(k, 2 * DFF)


def _gu_deinterleave(g):
    k = g.shape[0]
    return jnp.transpose(g.reshape(k, DFF // LANE, 2, LANE), (0, 2, 1, 3)).reshape(k, 2 * DFF)


def _gu_to_kernel(stacked):
    k = stacked.shape[1]
    return jnp.transpose(stacked.reshape(2, 2, k, GU_BLK, LANE), (2, 1, 3, 0, 4)).reshape(k, 2 * DFF)


def _gu_from_kernel(g):
    k = g.shape[0]
    return jnp.transpose(g.reshape(k, 2, GU_BLK, 2, LANE), (3, 1, 0, 2, 4)).reshape(N_CHIP, k, 2 * DFF // N_CHIP)


def _to_rows(v, width=SLAB_W):
    flat = v.reshape(-1)
    rows = -(-flat.shape[0] // width)
    return jnp.pad(flat, (0, rows * width - flat.shape[0])).reshape(rows, width)


def _pack_small(parts, names=SMALL):
    rows, spans, at = [], {}, 0
    for name in names:
        r = _to_rows(parts[name])
        spans[name] = (at, parts[name].size, parts[name].shape)
        rows.append(r)
        at += r.shape[0]
    pad = (-at) % SUBLANE
    if pad:
        rows.append(jnp.zeros((pad, SLAB_W), F32))
    return jnp.concatenate(rows, axis=0), spans


def _unpack_small(slab, spans):
    out = {}
    for name, (at, size, shape) in spans.items():
        nrow = -(-size // SLAB_W)
        out[name] = slab[at:at + nrow].reshape(-1)[:size].reshape(shape)
    return out


def kernel(x, c, positions, w_ada, b_ada, norm_ffn1, w_ffn1_in, w_ffn1_out, norm_mix, w_in, pool_grp, pool_scale, w_pool_proj, q_a_norm, w_q_up, kv_a_norm, w_kv_up, q_norm_nope, q_norm_rope, k_norm_nope, k_norm_rope, w_mla_proj, w_out, norm_ffn2, w_ffn2_in, w_ffn2_out, loss_target, m_w_ada, m_b_ada, m_norm_ffn1, m_w_ffn1_in, m_w_ffn1_out, m_norm_mix, m_w_in, m_pool_grp, m_pool_scale, m_w_pool_proj, m_q_a_norm, m_w_q_up, m_kv_a_norm, m_w_kv_up, m_q_norm_nope, m_q_norm_rope, m_k_norm_nope, m_k_norm_rope, m_w_mla_proj, m_w_out, m_norm_ffn2, m_w_ffn2_in, m_w_ffn2_out, v_w_ada, v_b_ada, v_norm_ffn1, v_w_ffn1_in, v_w_ffn1_out, v_norm_mix, v_w_in, v_pool_grp, v_pool_scale, v_w_pool_proj, v_q_a_norm, v_w_q_up, v_kv_a_norm, v_w_kv_up, v_q_norm_nope, v_q_norm_rope, v_k_norm_nope, v_k_norm_rope, v_w_mla_proj, v_w_out, v_norm_ffn2, v_w_ffn2_in, v_w_ffn2_out):
    args = dict(locals())
    wts = {n: args[n][0] for n in WEIGHTS}
    mom = {n: args["m_" + n][0] for n in WEIGHTS}
    var = {n: args["v_" + n][0] for n in WEIGHTS}
    nb, seq, dm = x.shape
    tokens = nb * seq
    xi, yi, ci = _coords()
    chip = 2 * xi + yi
    dev = 4 * xi + 2 * yi + ci

    c_all = _gather8(c, name="gather_c").reshape(N_DEV * nb, dm)
    ncol = w_ada.shape[2]
    b_cols = lax.dynamic_slice_in_dim(wts["b_ada"].reshape(1, -1), chip * ncol, ncol, axis=1)
    mod_cols = _ada_fwd(c_all, wts["w_ada"], b_cols, name="ada_fwd")
    mod_all = _gather8(mod_cols, name="gather_mod")
    mine = lax.dynamic_slice_in_dim(mod_all, dev * nb, nb, axis=1)
    mod = jnp.concatenate([mine[0], mine[2], mine[4], mine[6]], axis=1)
    mod3 = mod.reshape(nb, 9, dm)

    gathered = _gather_weights([wts[n].astype(MX) for n in BIG], name="gather_weights")
    full = {n: _assemble(n, g) for n, g in zip(BIG, gathered) if n not in GU}
    W = {KERNEL_NAME[n]: full[n] for n in BIG if n not in GU}
    for n, g in zip(BIG, gathered):
        if n in GU:
            W[KERNEL_NAME[n]] = _gu_to_kernel(g)
    W["w_in"] = _w_in_to_kernel(full["w_in"])
    W["q_up"] = _q_up_to_kernel(full["w_q_up"])
    W["kv_up"] = _kv_up_to_kernel(full["w_kv_up"])
    W["pool_grp"] = wts["pool_grp"].astype(MX)
    P = {"norm_ffn1": wts["norm_ffn1"].reshape(1, dm), "norm_mix": wts["norm_mix"].reshape(1, dm),
         "norm_ffn2": wts["norm_ffn2"].reshape(1, dm), "pool_scale": wts["pool_scale"].reshape(1, POOL_W),
         "q_a_norm": wts["q_a_norm"].reshape(1, QL), "kv_a_norm": wts["kv_a_norm"].reshape(1, KVL),
         "q_gain": _gain_slab(wts["q_norm_nope"].reshape(1, NOPE), wts["q_norm_rope"].reshape(1, ROPE)),
         "k_gain": _gain_slab(wts["k_norm_nope"].reshape(1, NOPE), wts["k_norm_rope"].reshape(1, ROPE))}
    cos, sin = _rope_tables(positions.reshape(tokens))

    loss8, grad_x, G, small, dmod = _layer_fwd_bwd(x.reshape(tokens, dm), loss_target.reshape(tokens, dm), mod3, cos, sin, W, P)
    loss = lax.psum(loss8[0, 0], ("x", "y", "c"))

    gfull = {KERNEL_NAME[n]: G[KERNEL_NAME[n]] for n in BIG}
    gfull["w_in"] = _w_in_from_kernel(G["w_in"])
    gfull["q_up"] = _q_up_from_kernel(G["q_up"])
    gfull["kv_up"] = _kv_up_from_kernel(G["kv_up"])
    parts = [_gu_from_kernel(G[KERNEL_NAME[n]]) if n in GU else _split(n, gfull[KERNEL_NAME[n]]) for n in BIG]
    cidx = ci.astype(jnp.int32).reshape(1)
    from_sib = _swap_halves(parts, name="rs_swap_halves")
    pre = [_add_half(p, o, cidx, name="rs_add_" + KERNEL_NAME[n]) for n, p, o in zip(BIG, parts, from_sib)]
    landed = _exchange_chips(pre, name="rs_exchange_chips")
    halves = [_sum_chips(q, cidx, name="rs_sum_" + KERNEL_NAME[n]) for n, q in zip(BIG, landed)]
    joined = _join_halves(halves, name="rs_join_halves")
    grads = {n: g for n, g in zip(BIG, joined)}

    qg, kg = small["q_gain"], small["k_gain"]
    per_seq = {"b_ada": dmod.reshape(nb, 9 * dm), "norm_ffn1": small["norm_ffn1"], "norm_mix": small["norm_mix"],
               "pool_scale": small["pool_scale"], "q_a_norm": small["q_a_norm"],
               "kv_a_norm": small["kv_a_norm"], "q_norm_nope": qg[:, :NOPE], "q_norm_rope": qg[:, NOPE:NOPE + ROPE],
               "k_norm_nope": kg[:, :NOPE], "k_norm_rope": kg[:, KR_LANE:KR_LANE + ROPE], "norm_ffn2": small["norm_ffn2"]}
    slabs, spans = [], None
    for s in range(nb):
        slab, spans = _pack_small({n: per_seq[n][s] for n in SMALL_SEQ}, SMALL_SEQ)
        slabs.append(slab)
    rows = slabs[0].shape[0]
    slabs.append(_to_rows(small["pool_grp"]))
    gathered_small = _gather8(jnp.concatenate(slabs, axis=0), name="gather_small")
    small_sum = _sum_small(gathered_small, nb, rows, name="sum_small")
    small_grads = _unpack_small(small_sum[:rows], spans)
    small_grads["pool_grp"] = small_sum[rows:]
    for n in SMALL:
        grads[n] = small_grads[n].reshape(wts[n].shape)
    at, _, _ = spans["b_ada"]
    nrow_b = 9 * dm // SLAB_W
    dmod_all = gathered_small[:, :nb * rows].reshape(N_DEV * nb, rows, SLAB_W)[:, at:at + nrow_b].reshape(N_DEV * nb, 9 * dm)
    dmod_cols = lax.dynamic_slice_in_dim(dmod_all, chip * ncol, ncol, axis=1)
    grads["w_ada"] = _ada_bwd(c_all, dmod_cols, name="ada_bwd")

    delta, new_m, new_v = {}, {}, {}
    for n in ("w_ada",) + BIG:
        delta[n], new_m[n], new_v[n] = _adamw(wts[n], grads[n], mom[n], var[n], name="adamw_" + n)
    w_slab, sp = _pack_small({n: wts[n] for n in SMALL})
    g_slab, _ = _pack_small({n: grads[n] for n in SMALL})
    m_slab, _ = _pack_small({n: mom[n] for n in SMALL})
    v_slab, _ = _pack_small({n: var[n] for n in SMALL})
    d_s, m_s, v_s = _adamw(w_slab, g_slab, m_slab, v_slab, name="adamw_small")
    for dst, slab in ((delta, d_s), (new_m, m_s), (new_v, v_s)):
        un = _unpack_small(slab, sp)
        for n in SMALL:
            dst[n] = un[n]

    def lead(a, n):
        return a.reshape((1,) + wts[n].shape)

    return (loss, grad_x.reshape(nb, seq, dm), *[lead(grads[n], n) for n in WEIGHTS], *[lead(delta[n], n) for n in WEIGHTS],
            *[lead(new_m[n], n) for n in WEIGHTS], *[lead(new_v[n], n) for n in WEIGHTS])
```

```python
import functools
import math

import jax
import jax.numpy as jnp
from jax import lax
from jax.experimental import pallas as pl
from jax.experimental.pallas import tpu as pltpu

F32 = jnp.float32
MX = jnp.bfloat16

D = 1024
DFF = 2816
NH = 8
POOL_W = 512
POOL_G = 4
QL = 384
KVL = 256
ROPE = 32
NOPE = 64
LANE = 128
SUBLANE = 8
EPS = 1e-6
ATTN_SCALE = 1.0 / math.sqrt(96.0)
NEG = -1e30

Z_UP, Z_QL, Z_KV, Z_KR, Z_GP, Z_GM, Z_W = 0, 512, 896, 1152, 1280, 2304, 3328
KR_LANE = 64
LAT_W = 1280

ADAM_LR, ADAM_B1, ADAM_B2, ADAM_EPS, ADAM_WD, ADAM_STEP = 0.001, 0.9, 0.999, 1e-08, 0.01, 10

VMEM_LIMIT = 48 * 1024 * 1024
MESH = pl.DeviceIdType.MESH
N_DEV = 8
N_CHIP = 4


def _pcall(body, *, name, out_shape, grid=(), in_specs=None, out_specs=None, scratch=(), grid_spec=None, aliases=None):
    params = pltpu.CompilerParams(vmem_limit_bytes=VMEM_LIMIT)
    kw = dict(name=name, out_shape=out_shape, compiler_params=params)
    if aliases:
        kw["input_output_aliases"] = aliases
    if grid_spec is not None:
        return pl.pallas_call(body, grid_spec=grid_spec, **kw)
    return pl.pallas_call(body, grid=grid, in_specs=in_specs, out_specs=out_specs, scratch_shapes=scratch, **kw)


def _pick(dim, target):
    best = None
    for t in range(LANE, min(dim, target) + 1, LANE):
        if dim % t == 0:
            best = t
    return dim if best is None else best


def _sds(shape, dtype):
    return jax.ShapeDtypeStruct(shape, dtype)


def _mm(a, b, *, mode, out_dtype, name, tm=512, tn=512, tk=1024, n_outer=False):
    if mode == "nn":
        (M, K), (K2, N) = a.shape, b.shape
    elif mode == "nt":
        (M, K), (N, K2) = a.shape, b.shape
    else:
        (K, M), (K2, N) = a.shape, b.shape
    assert K == K2, (name, a.shape, b.shape)
    tm, tn, tk = _pick(M, tm), _pick(N, tn), _pick(K, tk)
    nk = K // tk
    if n_outer:
        ij = lambda g0, g1: (g1, g0)
        grid = (N // tn, M // tm, nk)
    else:
        ij = lambda g0, g1: (g0, g1)
        grid = (M // tm, N // tn, nk)
    if mode == "tn":
        a_spec = pl.BlockSpec((tk, tm), lambda g0, g1, k: (k, ij(g0, g1)[0]))
    else:
        a_spec = pl.BlockSpec((tm, tk), lambda g0, g1, k: (ij(g0, g1)[0], k))
    if mode == "nt":
        b_spec = pl.BlockSpec((tn, tk), lambda g0, g1, k: (ij(g0, g1)[1], k))
    else:
        b_spec = pl.BlockSpec((tk, tn), lambda g0, g1, k: (k, ij(g0, g1)[1]))
    o_spec = pl.BlockSpec((tm, tn), lambda g0, g1, k: ij(g0, g1))
    dn = {"nn": (((1,), (0,)), ((), ())), "nt": (((1,), (1,)), ((), ())), "tn": (((0,), (0,)), ((), ()))}[mode]

    def dot(a_ref, b_ref):
        return lax.dot_general(a_ref[...].astype(MX), b_ref[...].astype(MX), dn, preferred_element_type=F32)

    def body_one(a_ref, b_ref, o_ref):
        o_ref[...] = dot(a_ref, b_ref).astype(o_ref.dtype)

    def body_acc(a_ref, b_ref, o_ref, acc_ref):
        k = pl.program_id(2)
        part = dot(a_ref, b_ref)

        @pl.when(k == 0)
        def _():
            acc_ref[...] = part

        @pl.when(k > 0)
        def _():
            acc_ref[...] += part

        @pl.when(k == nk - 1)
        def _():
            o_ref[...] = acc_ref[...].astype(o_ref.dtype)

    return _pcall(body_one if nk == 1 else body_acc, name=name, out_shape=_sds((M, N), out_dtype), grid=grid,
                  in_specs=[a_spec, b_spec], out_specs=o_spec, scratch=[] if nk == 1 else [pltpu.VMEM((tm, tn), F32)])(a, b)


def _gu_shard(q):
    return (q % 2) * 2 + q // 2


def _ffn_up(h, w_st, *, name, tm=512):
    T, dm = h.shape
    hw = w_st.shape[2]
    tm = _pick(T, tm)

    def body(h_ref, wg_ref, wu_ref, gu_ref, a_ref):
        hv = h_ref[...]
        g = jnp.dot(hv, wg_ref[0], preferred_element_type=F32)
        u = jnp.dot(hv, wu_ref[0], preferred_element_type=F32)
        gu_ref[:, :hw] = g.astype(gu_ref.dtype)
        gu_ref[:, hw:] = u.astype(gu_ref.dtype)
        a_ref[...] = (g * _sigmoid(g) * u).astype(a_ref.dtype)

    return _pcall(body, name=name, grid=(T // tm, 2),
                  in_specs=[pl.BlockSpec((tm, dm), lambda i, j: (i, 0)), pl.BlockSpec((1, dm, hw), lambda i, j: (j, 0, 0)),
                            pl.BlockSpec((1, dm, hw), lambda i, j: (2 + j, 0, 0))],
                  out_specs=[pl.BlockSpec((tm, 2 * hw), lambda i, j: (i, j)), pl.BlockSpec((tm, hw), lambda i, j: (i, j))],
                  out_shape=[_sds((T, 4 * hw), MX), _sds((T, 2 * hw), MX)])(h, w_st, w_st)


def _ffn_dact(df, gu, w_out, *, name, tm=512):
    T, dm = df.shape
    hw = gu.shape[1] // 4
    tm = _pick(T, tm)

    def body(df_ref, gu_ref, wo_ref, dgu_ref):
        da = lax.dot_general(df_ref[...], wo_ref[...], (((1,), (1,)), ((), ())), preferred_element_type=F32)
        g = gu_ref[:, :hw].astype(F32)
        u = gu_ref[:, hw:].astype(F32)
        s = _sigmoid(g)
        dgu_ref[:, :hw] = (da * u * (s * (1.0 + g * (1.0 - s)))).astype(dgu_ref.dtype)
        dgu_ref[:, hw:] = (da * (g * s)).astype(dgu_ref.dtype)

    return _pcall(body, name=name, grid=(T // tm, 2),
                  in_specs=[pl.BlockSpec((tm, dm), lambda i, j: (i, 0)), pl.BlockSpec((tm, 2 * hw), lambda i, j: (i, j)),
                            pl.BlockSpec((hw, dm), lambda i, j: (j, 0))],
                  out_specs=pl.BlockSpec((tm, 2 * hw), lambda i, j: (i, j)), out_shape=_sds(gu.shape, MX))(df, gu, w_out)


def _ffn_dh(dgu, w_st, *, name, tm=1024):
    T = dgu.shape[0]
    _, dm, hw = w_st.shape
    tm = _pick(T, tm)

    def body(d_ref, w_ref, o_ref, acc_ref):
        q = pl.program_id(1)
        part = lax.dot_general(d_ref[...], w_ref[0], (((1,), (1,)), ((), ())), preferred_element_type=F32)

        @pl.when(q == 0)
        def _():
            acc_ref[...] = part

        @pl.when(q > 0)
        def _():
            acc_ref[...] += part

        @pl.when(q == 3)
        def _():
            o_ref[...] = acc_ref[...]

    return _pcall(body, name=name, grid=(T // tm, 4),
                  in_specs=[pl.BlockSpec((tm, hw), lambda i, q: (i, q)), pl.BlockSpec((1, dm, hw), lambda i, q: (_gu_shard(q), 0, 0))],
                  out_specs=pl.BlockSpec((tm, dm), lambda i, q: (i, 0)), out_shape=_sds((T, dm), F32),
                  scratch=[pltpu.VMEM((tm, dm), F32)])(dgu, w_st)


def _ffn_dw_in(h, dgu, *, name, tm=256):
    T, dm = h.shape
    hw = dgu.shape[1] // 4
    tm = _pick(dm, tm)

    def body(h_ref, d_ref, o_ref):
        o_ref[0] = lax.dot_general(h_ref[...], d_ref[...], (((0,), (0,)), ((), ())), preferred_element_type=F32)

    return _pcall(body, name=name, grid=(4, dm // tm),
                  in_specs=[pl.BlockSpec((T, tm), lambda q, i: (0, i)), pl.BlockSpec((T, hw), lambda q, i: (0, q))],
                  out_specs=pl.BlockSpec((1, tm, hw), lambda q, i: (_gu_shard(q), i, 0)),
                  out_shape=_sds((4, dm, hw), F32))(h, dgu)


def _rsq(x):
    return lax.rsqrt(jnp.mean(x * x, axis=-1, keepdims=True) + EPS)


def _sigmoid(x):
    return 1.0 / (1.0 + jnp.exp(-x))


def _row_spec(tm, w):
    return pl.BlockSpec((tm, w), lambda i: (i, 0))


def _fwd_block(x_prev, f_prev, mod3, gain, *, sub, coef, name, tm=256):
    T, dm = x_prev.shape
    tps = (T // mod3.shape[0]) // tm
    has_f = f_prev is not None
    mod_spec = pl.BlockSpec((1, 9, dm), lambda i: (i // tps, 0, 0))
    vec_spec = pl.BlockSpec((1, dm), lambda i: (0, 0))

    def body(*refs):
        if has_f:
            x_ref, f_ref, mod_ref, n_ref, xo_ref, h_ref = refs
            x = x_ref[...] + coef * mod_ref[0, 3 * sub - 1:3 * sub, :] * f_ref[...]
            xo_ref[...] = x
        else:
            x_ref, mod_ref, n_ref, h_ref = refs
            x = x_ref[...]
        xn = x * _rsq(x) * n_ref[...]
        h = xn * (1.0 + mod_ref[0, 3 * sub + 1:3 * sub + 2, :]) + mod_ref[0, 3 * sub:3 * sub + 1, :]
        h_ref[...] = h.astype(h_ref.dtype)

    row = _row_spec(tm, dm)
    if has_f:
        return _pcall(body, name=name, grid=(T // tm,), in_specs=[row, row, mod_spec, vec_spec], out_specs=[row, row],
                      out_shape=[_sds((T, dm), F32), _sds((T, dm), MX)])(x_prev, f_prev, mod3, gain)
    return _pcall(body, name=name, grid=(T // tm,), in_specs=[row, mod_spec, vec_spec], out_specs=row,
                  out_shape=_sds((T, dm), MX))(x_prev, mod3, gain)


def _final(x2, f2, tgt, mod3, *, name, tm=256):
    T, dm = x2.shape
    tps = (T // mod3.shape[0]) // tm
    mod_spec = pl.BlockSpec((1, 9, dm), lambda i: (i // tps, 0, 0))
    row = _row_spec(tm, dm)
    stat_spec = pl.BlockSpec((1, SUBLANE, dm), lambda i: (i // tps, 0, 0))
    loss_spec = pl.BlockSpec((SUBLANE, LANE), lambda i: (0, 0))

    def body(x_ref, f_ref, t_ref, mod_ref, dy_ref, df_ref, st_ref, loss_ref):
        i = pl.program_id(0)
        g = mod_ref[0, 8:9, :]
        f = f_ref[...]
        err = x_ref[...] + 0.5 * g * f - t_ref[...]
        dy = err * (1.0 / dm)
        dy_ref[...] = dy
        df_ref[...] = (0.5 * g * dy).astype(df_ref.dtype)
        dgate = jnp.sum(0.5 * dy * f, axis=0, keepdims=True)
        part = 0.5 * jnp.sum(jnp.sum(err * err, axis=0, keepdims=True), axis=1, keepdims=True) * (1.0 / dm)

        @pl.when(i % tps == 0)
        def _():
            st_ref[...] = jnp.zeros_like(st_ref)

        @pl.when(i == 0)
        def _():
            loss_ref[...] = jnp.zeros_like(loss_ref)

        st_ref[0, 0:1, :] += dgate
        loss_ref[...] += jnp.broadcast_to(part, loss_ref.shape)

    return _pcall(body, name=name, grid=(T // tm,), in_specs=[row, row, row, mod_spec],
                  out_specs=[row, row, stat_spec, loss_spec],
                  out_shape=[_sds((T, dm), F32), _sds((T, dm), MX), _sds((mod3.shape[0], SUBLANE, dm), F32),
                             _sds((SUBLANE, LANE), F32)])(x2, f2, tgt, mod3)


def _bwd_block(x_cur, dh, dx_in, f_prev, mod3, gain, *, sub, coef, name, tm=256):
    T, dm = x_cur.shape
    nb = mod3.shape[0]
    tps = (T // nb) // tm
    has_f = f_prev is not None
    mod_spec = pl.BlockSpec((1, 9, dm), lambda i: (i // tps, 0, 0))
    vec_spec = pl.BlockSpec((1, dm), lambda i: (0, 0))
    stat_spec = pl.BlockSpec((1, SUBLANE, dm), lambda i: (i // tps, 0, 0))
    row = _row_spec(tm, dm)

    def body(*refs):
        if has_f:
            x_ref, dh_ref, dxi_ref, f_ref, mod_ref, n_ref, dx_ref, df_ref, st_ref = refs
        else:
            x_ref, dh_ref, dxi_ref, mod_ref, n_ref, dx_ref, st_ref = refs
        i = pl.program_id(0)
        x = x_ref[...]
        r = _rsq(x)
        xhat = x * r
        n = n_ref[...]
        dhv = dh_ref[...]
        d_shift = jnp.sum(dhv, axis=0, keepdims=True)
        d_scale = jnp.sum(dhv * (xhat * n), axis=0, keepdims=True)
        dxn = dhv * (1.0 + mod_ref[0, 3 * sub + 1:3 * sub + 2, :])
        d_gain = jnp.sum(dxn * xhat, axis=0, keepdims=True)
        dxhat = dxn * n
        dx = dxi_ref[...] + r * (dxhat - xhat * jnp.mean(dxhat * xhat, axis=-1, keepdims=True))
        dx_ref[...] = dx

        @pl.when(i % tps == 0)
        def _():
            st_ref[...] = jnp.zeros_like(st_ref)

        st_ref[0, 0:1, :] += d_shift
        st_ref[0, 1:2, :] += d_scale
        st_ref[0, 2:3, :] += d_gain
        if has_f:
            f = f_ref[...]
            st_ref[0, 3:4, :] += jnp.sum(coef * dx * f, axis=0, keepdims=True)
            df_ref[...] = (coef * mod_ref[0, 3 * sub - 1:3 * sub, :] * dx).astype(df_ref.dtype)

    st_shape = _sds((nb, SUBLANE, dm), F32)
    if has_f:
        return _pcall(body, name=name, grid=(T // tm,), in_specs=[row, row, row, row, mod_spec, vec_spec],
                      out_specs=[row, row, stat_spec],
                      out_shape=[_sds((T, dm), F32), _sds((T, dm), MX), st_shape])(x_cur, dh, dx_in, f_prev, mod3, gain)
    return _pcall(body, name=name, grid=(T // tm,), in_specs=[row, row, row, mod_spec, vec_spec],
                  out_specs=[row, stat_spec], out_shape=[_sds((T, dm), F32), st_shape])(x_cur, dh, dx_in, mod3, gain)


def _merge_fwd(z, br_pool, br_mla, *, name, tm=256):
    T = z.shape[0]

    def body(z_ref, bp_ref, bm_ref, o_ref):
        gp = z_ref[:, Z_GP:Z_GP + D]
        gm = z_ref[:, Z_GM:Z_GM + D]
        o_ref[...] = (_sigmoid(gp) * bp_ref[...] + _sigmoid(gm) * bm_ref[...]).astype(o_ref.dtype)

    return _pcall(body, name=name, grid=(T // tm,), in_specs=[_row_spec(tm, Z_W), _row_spec(tm, D), _row_spec(tm, D)],
                  out_specs=_row_spec(tm, D), out_shape=_sds((T, D), MX))(z, br_pool, br_mla)


def _merge_bwd(z, br_pool, br_mla, dmerged, *, name, tm=256):
    T = z.shape[0]

    def body(z_ref, bp_ref, bm_ref, dm_ref, dbp_ref, dbm_ref, dg_ref):
        dm = dm_ref[...]
        sp = _sigmoid(z_ref[:, Z_GP:Z_GP + D])
        sm = _sigmoid(z_ref[:, Z_GM:Z_GM + D])
        dbp_ref[...] = (dm * sp).astype(dbp_ref.dtype)
        dbm_ref[...] = (dm * sm).astype(dbm_ref.dtype)
        dg_ref[:, :D] = (dm * bp_ref[...] * sp * (1.0 - sp)).astype(dg_ref.dtype)
        dg_ref[:, D:] = (dm * bm_ref[...] * sm * (1.0 - sm)).astype(dg_ref.dtype)

    return _pcall(body, name=name, grid=(T // tm,),
                  in_specs=[_row_spec(tm, Z_W), _row_spec(tm, D), _row_spec(tm, D), _row_spec(tm, D)],
                  out_specs=[_row_spec(tm, D), _row_spec(tm, D), _row_spec(tm, 2 * D)],
                  out_shape=[_sds((T, D), MX), _sds((T, D), MX), _sds((T, 2 * D), MX)])(z, br_pool, br_mla, dmerged)


def _shift_down(x, k, row):
    return jnp.where(row >= k, pltpu.roll(x, k, 0), 0.0)


def _shift_up(x, k, row, n):
    return jnp.where(row < n - k, pltpu.roll(x, n - k, 0), 0.0)


def _pool_fwd(z, pool_grp, pool_scale, *, nb, name):
    T = z.shape[0]
    S = T // nb
    blk = pl.BlockSpec((S, LANE), lambda b, g: (b, g))

    def body(u_ref, w_ref, s_ref, pooled_ref, mixed_ref, scaled_ref):
        g = pl.program_id(1)
        u = u_ref[...]
        row = lax.broadcasted_iota(jnp.int32, u.shape, 0)
        s2 = u + _shift_down(u, 1, row)
        s4 = s2 + _shift_down(s2, 2, row)
        s8 = s4 + _shift_down(s4, 4, row)
        s16 = s8 + _shift_down(s8, 8, row)
        win = jnp.where(g == 0, s2, jnp.where(g == 1, s4, jnp.where(g == 2, s8, s16)))
        width = lax.shift_left(jnp.int32(2), g)
        cnt = jnp.minimum(row + 1, width).astype(F32)
        pooled = (win / cnt - u).astype(MX)
        pooled_ref[...] = pooled
        mixed = jnp.dot(pooled, w_ref[0], preferred_element_type=F32)
        mixed_ref[...] = mixed
        scaled_ref[...] = (mixed * s_ref[...]).astype(scaled_ref.dtype)

    return _pcall(body, name=name, grid=(nb, POOL_G),
                  in_specs=[blk, pl.BlockSpec((1, LANE, LANE), lambda b, g: (g, 0, 0)),
                            pl.BlockSpec((1, LANE), lambda b, g: (0, g))],
                  out_specs=[blk, blk, blk],
                  out_shape=[_sds((T, POOL_W), MX), _sds((T, POOL_W), F32), _sds((T, POOL_W), MX)])(z, pool_grp, pool_scale)


def _pool_bwd(dscaled, mixed, pooled, pool_grp, pool_scale, *, nb, name):
    T = dscaled.shape[0]
    S = T // nb
    blk = pl.BlockSpec((S, LANE), lambda g, b: (b, g))

    def body(ds_ref, mixed_ref, pooled_ref, w_ref, s_ref, du_ref, dw_ref, dsc_ref):
        g = pl.program_id(0)
        b = pl.program_id(1)
        ds = ds_ref[...]
        dsc_ref[0] = jnp.sum(ds * mixed_ref[...], axis=0, keepdims=True)
        dmixed = (ds * s_ref[...]).astype(MX)
        dw = lax.dot_general(pooled_ref[...], dmixed, (((0,), (0,)), ((), ())), preferred_element_type=F32)

        @pl.when(b == 0)
        def _():
            dw_ref[0] = dw

        @pl.when(b > 0)
        def _():
            dw_ref[0] += dw
        dpooled = lax.dot_general(dmixed, w_ref[0], (((1,), (1,)), ((), ())), preferred_element_type=F32)
        row = lax.broadcasted_iota(jnp.int32, dpooled.shape, 0)
        width = lax.shift_left(jnp.int32(2), g)
        q = dpooled / jnp.minimum(row + 1, width).astype(F32)
        r2 = q + _shift_up(q, 1, row, S)
        r4 = r2 + _shift_up(r2, 2, row, S)
        r8 = r4 + _shift_up(r4, 4, row, S)
        r16 = r8 + _shift_up(r8, 8, row, S)
        win = jnp.where(g == 0, r2, jnp.where(g == 1, r4, jnp.where(g == 2, r8, r16)))
        du_ref[...] = (win - dpooled).astype(du_ref.dtype)

    return _pcall(body, name=name, grid=(POOL_G, nb),
                  in_specs=[blk, blk, blk, pl.BlockSpec((1, LANE, LANE), lambda g, b: (g, 0, 0)),
                            pl.BlockSpec((1, LANE), lambda g, b: (0, g))],
                  out_specs=[blk, pl.BlockSpec((1, LANE, LANE), lambda g, b: (g, 0, 0)),
                             pl.BlockSpec((1, 1, LANE), lambda g, b: (b, 0, g))],
                  out_shape=[_sds((T, POOL_W), MX), _sds((POOL_G, LANE, LANE), F32), _sds((nb, 1, POOL_W), F32)],
                  )(dscaled, mixed, pooled, pool_grp, pool_scale)


def _lat_fwd(z, q_gain, kv_gain, *, name, tm=256):
    T = z.shape[0]

    def body(z_ref, qg_ref, kg_ref, qn_ref, kvn_ref):
        ql = z_ref[:, Z_QL:Z_QL + QL]
        kv = z_ref[:, Z_KV:Z_KV + KVL]
        qn_ref[...] = (ql * _rsq(ql) * qg_ref[...]).astype(qn_ref.dtype)
        kvn_ref[...] = (kv * _rsq(kv) * kg_ref[...]).astype(kvn_ref.dtype)

    return _pcall(body, name=name, grid=(T // tm,),
                  in_specs=[_row_spec(tm, LAT_W), pl.BlockSpec((1, QL), lambda i: (0, 0)), pl.BlockSpec((1, KVL), lambda i: (0, 0))],
                  out_specs=[_row_spec(tm, QL), _row_spec(tm, KVL)],
                  out_shape=[_sds((T, QL), MX), _sds((T, KVL), MX)])(z, q_gain, kv_gain)


def _lat_bwd(z, dqn, dkvn, q_gain, kv_gain, *, nb, name, tm=256):
    T = z.shape[0]
    tps = (T // nb) // tm

    def norm_bwd(x, dy, gain):
        r = _rsq(x)
        xhat = x * r
        dgain = jnp.sum(dy * xhat, axis=0, keepdims=True)
        dxhat = dy * gain
        return r * (dxhat - xhat * jnp.mean(dxhat * xhat, axis=-1, keepdims=True)), dgain

    def body(z_ref, dq_ref, dkv_ref, qg_ref, kg_ref, dql_ref, dkvl_ref, sq_ref, sk_ref):
        i = pl.program_id(0)
        dql, dqg = norm_bwd(z_ref[:, Z_QL:Z_QL + QL], dq_ref[...], qg_ref[...])
        dkvl, dkg = norm_bwd(z_ref[:, Z_KV:Z_KV + KVL], dkv_ref[...], kg_ref[...])
        dql_ref[...] = dql.astype(dql_ref.dtype)
        dkvl_ref[...] = dkvl.astype(dkvl_ref.dtype)

        @pl.when(i % tps == 0)
        def _():
            sq_ref[...] = jnp.zeros_like(sq_ref)
            sk_ref[...] = jnp.zeros_like(sk_ref)

        sq_ref[0, 0:1, :] += dqg
        sk_ref[0, 0:1, :] += dkg

    return _pcall(body, name=name, grid=(T // tm,),
                  in_specs=[_row_spec(tm, LAT_W), _row_spec(tm, QL), _row_spec(tm, KVL),
                            pl.BlockSpec((1, QL), lambda i: (0, 0)), pl.BlockSpec((1, KVL), lambda i: (0, 0))],
                  out_specs=[_row_spec(tm, QL), _row_spec(tm, KVL),
                             pl.BlockSpec((1, SUBLANE, QL), lambda i: (i // tps, 0, 0)),
                             pl.BlockSpec((1, SUBLANE, KVL), lambda i: (i // tps, 0, 0))],
                  out_shape=[_sds((T, QL), MX), _sds((T, KVL), MX), _sds((nb, SUBLANE, QL), F32), _sds((nb, SUBLANE, KVL), F32)],
                  )(z, dqn, dkvn, q_gain, kv_gain)


def _lane_masks(shape):
    lane = lax.broadcasted_iota(jnp.int32, shape, len(shape) - 1)
    m_n = lane < NOPE
    m_r = jnp.logical_and(lane >= KR_LANE, lane < KR_LANE + ROPE)
    first_half = lane < KR_LANE + ROPE // 2
    return m_n, m_r, first_half


def _rot(y, first_half):
    return jnp.where(first_half, -pltpu.roll(y, LANE - ROPE // 2, 1), pltpu.roll(y, ROPE // 2, 1))


def _rot_t(v, first_half, m_r):
    return jnp.where(m_r, jnp.where(first_half, pltpu.roll(v, LANE - ROPE // 2, 1), -pltpu.roll(v, ROPE // 2, 1)), 0.0)


def _prep_fwd(qp, kvp, z, cos, sin, q_gain, k_gain, *, name, tm=256):
    T = qp.shape[0]
    slab = pl.BlockSpec((tm, LANE), lambda i: (i, 0))
    kr_spec = pl.BlockSpec((tm, LANE), lambda i: (i, Z_KR // LANE))
    vec = pl.BlockSpec((1, LANE), lambda i: (0, 0))

    def body(qp_ref, kvp_ref, kr_ref, cos_ref, sin_ref, qg_ref, kg_ref, q_ref, k_ref, v_ref):
        m_n, m_r, first_half = _lane_masks((tm, LANE))
        c = cos_ref[...]
        s = sin_ref[...]
        qg = qg_ref[...]
        kg = kg_ref[...]
        xr = kr_ref[...]
        rr = lax.rsqrt(jnp.sum(xr * xr, axis=-1, keepdims=True) * (1.0 / ROPE) + EPS)
        yr = xr * rr * kg
        kr = jnp.where(m_r, yr * c + _rot(yr, first_half) * s, 0.0)
        for h in range(NH):
            x = qp_ref[:, h * LANE:(h + 1) * LANE]
            x2 = x * x
            rn = lax.rsqrt(jnp.sum(jnp.where(m_n, x2, 0.0), axis=-1, keepdims=True) * (1.0 / NOPE) + EPS)
            rq = lax.rsqrt(jnp.sum(jnp.where(m_r, x2, 0.0), axis=-1, keepdims=True) * (1.0 / ROPE) + EPS)
            y = x * jnp.where(m_n, rn, jnp.where(m_r, rq, 0.0)) * qg
            q_ref[:, h * LANE:(h + 1) * LANE] = (y * c + _rot(y, first_half) * s).astype(q_ref.dtype)
            xk = kvp_ref[:, h * LANE:(h + 1) * LANE]
            rk = lax.rsqrt(jnp.sum(jnp.where(m_n, xk * xk, 0.0), axis=-1, keepdims=True) * (1.0 / NOPE) + EPS)
            k_ref[:, h * LANE:(h + 1) * LANE] = (jnp.where(m_n, xk * rk * kg, 0.0) + kr).astype(k_ref.dtype)
        v_ref[...] = kvp_ref[:, NH * LANE:].astype(v_ref.dtype)

    return _pcall(body, name=name, grid=(T // tm,),
                  in_specs=[_row_spec(tm, NH * LANE), _row_spec(tm, NH * LANE + NH * NOPE), kr_spec, slab, slab, vec, vec],
                  out_specs=[_row_spec(tm, NH * LANE), _row_spec(tm, NH * LANE), _row_spec(tm, NH * NOPE)],
                  out_shape=[_sds((T, NH * LANE), MX), _sds((T, NH * LANE), MX), _sds((T, NH * NOPE), MX)],
                  )(qp, kvp, z, cos, sin, q_gain, k_gain)


def _prep_bwd(dq, dk, dv, qp, kvp, z, cos, sin, q_gain, k_gain, *, nb, name, tm=256):
    T = qp.shape[0]
    tps = (T // nb) // tm
    slab = pl.BlockSpec((tm, LANE), lambda i: (i, 0))
    kr_spec = pl.BlockSpec((tm, LANE), lambda i: (i, Z_KR // LANE))
    vec = pl.BlockSpec((1, LANE), lambda i: (0, 0))

    def body(dq_ref, dk_ref, dv_ref, qp_ref, kvp_ref, kr_ref, cos_ref, sin_ref, qg_ref, kg_ref,
             dqp_ref, dkvp_ref, dkr_ref, st_ref):
        i = pl.program_id(0)
        m_n, m_r, first_half = _lane_masks((tm, LANE))
        c = cos_ref[...]
        s = sin_ref[...]
        qg = qg_ref[...]
        kg = kg_ref[...]
        dqg = jnp.zeros((1, LANE), F32)
        dkg = jnp.zeros((1, LANE), F32)
        dkr_sum = jnp.zeros((tm, LANE), F32)
        for h in range(NH):
            x = qp_ref[:, h * LANE:(h + 1) * LANE]
            x2 = x * x
            rn = lax.rsqrt(jnp.sum(jnp.where(m_n, x2, 0.0), axis=-1, keepdims=True) * (1.0 / NOPE) + EPS)
            rq = lax.rsqrt(jnp.sum(jnp.where(m_r, x2, 0.0), axis=-1, keepdims=True) * (1.0 / ROPE) + EPS)
            rfac = jnp.where(m_n, rn, jnp.where(m_r, rq, 0.0))
            xhat = x * rfac
            do = dq_ref[:, h * LANE:(h + 1) * LANE]
            dy = do * c + _rot_t(do * s, first_half, m_r)
            dqg = dqg + jnp.sum(dy * xhat, axis=0, keepdims=True)
            dxhat = dy * qg
            t = dxhat * xhat
            mean_n = jnp.sum(jnp.where(m_n, t, 0.0), axis=-1, keepdims=True) * (1.0 / NOPE)
            mean_r = jnp.sum(jnp.where(m_r, t, 0.0), axis=-1, keepdims=True) * (1.0 / ROPE)
            dqp_ref[:, h * LANE:(h + 1) * LANE] = (
                rfac * (dxhat - xhat * jnp.where(m_n, mean_n, jnp.where(m_r, mean_r, 0.0)))).astype(dqp_ref.dtype)

            xk = kvp_ref[:, h * LANE:(h + 1) * LANE]
            rk = lax.rsqrt(jnp.sum(jnp.where(m_n, xk * xk, 0.0), axis=-1, keepdims=True) * (1.0 / NOPE) + EPS)
            khat = jnp.where(m_n, xk * rk, 0.0)
            dko = dk_ref[:, h * LANE:(h + 1) * LANE]
            dkn = jnp.where(m_n, dko, 0.0)
            dkg = dkg + jnp.sum(dkn * khat, axis=0, keepdims=True)
            dkhat = dkn * kg
            mean_k = jnp.sum(dkhat * khat, axis=-1, keepdims=True) * (1.0 / NOPE)
            dkvp_ref[:, h * LANE:(h + 1) * LANE] = jnp.where(m_n, rk * (dkhat - khat * mean_k), 0.0).astype(dkvp_ref.dtype)
            dkr_sum = dkr_sum + jnp.where(m_r, dko, 0.0)
        dkvp_ref[:, NH * LANE:] = dv_ref[...].astype(dkvp_ref.dtype)

        xr = kr_ref[...]
        rr = lax.rsqrt(jnp.sum(xr * xr, axis=-1, keepdims=True) * (1.0 / ROPE) + EPS)
        rhat = xr * rr
        dyr = dkr_sum * c + _rot_t(dkr_sum * s, first_half, m_r)
        dkg = dkg + jnp.sum(dyr * rhat, axis=0, keepdims=True)
        drhat = dyr * kg
        mean_kr = jnp.sum(drhat * rhat, axis=-1, keepdims=True) * (1.0 / ROPE)
        dkr_ref[...] = jnp.where(m_r, rr * (drhat - rhat * mean_kr), 0.0).astype(dkr_ref.dtype)

        @pl.when(i % tps == 0)
        def _():
            st_ref[...] = jnp.zeros_like(st_ref)

        st_ref[0, 0:1, :] += dqg
        st_ref[0, 1:2, :] += dkg

    return _pcall(body, name=name, grid=(T // tm,),
                  in_specs=[_row_spec(tm, NH * LANE), _row_spec(tm, NH * LANE), _row_spec(tm, NH * NOPE),
                            _row_spec(tm, NH * LANE), _row_spec(tm, NH * LANE + NH * NOPE), kr_spec, slab, slab, vec, vec],
                  out_specs=[_row_spec(tm, NH * LANE), _row_spec(tm, NH * LANE + NH * NOPE), slab,
                             pl.BlockSpec((1, SUBLANE, LANE), lambda i: (i // tps, 0, 0))],
                  out_shape=[_sds((T, NH * LANE), MX), _sds((T, NH * LANE + NH * NOPE), MX), _sds((T, LANE), MX),
                             _sds((nb, SUBLANE, LANE), F32)],
                  )(dq, dk, dv, qp, kvp, z, cos, sin, q_gain, k_gain)


def _causal_mask(qi, kj, tq, tk):
    r = lax.broadcasted_iota(jnp.int32, (tq, tk), 0) + qi * tq
    c = lax.broadcasted_iota(jnp.int32, (tq, tk), 1) + kj * tk
    return c <= r


def _attn_fwd(q, k, v, *, nb, name, tq=256):
    T = q.shape[0]
    S = T // nb
    nq = S // tq
    tk = tq
    npair = NH // 2

    def body(q_ref, k_ref, v_ref, o_ref, lse_ref):
        qi = pl.program_id(2)
        lane = lax.broadcasted_iota(jnp.int32, (tq, LANE), 1)
        outs, lses = [], []
        for hh in range(2):
            qh = q_ref[:, hh * LANE:(hh + 1) * LANE]

            def step(j, carry, hh=hh, qh=qh):
                m, l, acc = carry
                kb = k_ref[pl.ds(pl.multiple_of(j * tk, tk), tk), hh * LANE:(hh + 1) * LANE]
                vb = v_ref[pl.ds(pl.multiple_of(j * tk, tk), tk), :]
                s = lax.dot_general(qh, kb, (((1,), (1,)), ((), ())), preferred_element_type=F32) * ATTN_SCALE
                s = jnp.where(_causal_mask(qi, j, tq, tk), s, NEG)
                m_new = jnp.maximum(m, jnp.max(s, axis=-1, keepdims=True))
                p = jnp.exp(s - m_new)
                alpha = jnp.exp(m - m_new)
                l = alpha * l + jnp.sum(p, axis=-1, keepdims=True)
                acc = alpha * acc + jnp.dot(p.astype(MX), vb, preferred_element_type=F32)
                return m_new, l, acc

            m, l, acc = lax.fori_loop(0, qi + 1, step, (jnp.full((tq, 1), NEG, F32), jnp.zeros((tq, 1), F32),
                                                        jnp.zeros((tq, LANE), F32)))
            outs.append(acc / l)
            lses.append(m + jnp.log(l))
        o_ref[...] = jnp.where(lane < NOPE, outs[0], outs[1]).astype(o_ref.dtype)
        lse_ref[...] = jnp.where(lane < NOPE, lses[0], lses[1])

    return _pcall(body, name=name, grid=(nb, npair, nq),
                  in_specs=[pl.BlockSpec((tq, 2 * LANE), lambda b, p, i: (b * nq + i, p)),
                            pl.BlockSpec((S, 2 * LANE), lambda b, p, i: (b, p)),
                            pl.BlockSpec((S, LANE), lambda b, p, i: (b, p))],
                  out_specs=[pl.BlockSpec((tq, LANE), lambda b, p, i: (b * nq + i, p)),
                             pl.BlockSpec((tq, LANE), lambda b, p, i: (b * nq + i, p))],
                  out_shape=[_sds((T, NH * NOPE), MX), _sds((T, NH * NOPE), F32)])(q, k, v)


def _attn_bwd(q, k, v, o, lse, do, *, nb, name, tq=256):
    T = q.shape[0]
    S = T // nb
    nq = S // tq
    tk = tq
    npair = NH // 2

    def body(q_ref, k_ref, v_ref, o_ref, lse_ref, do_ref, dq_ref, dk_ref, dv_ref):
        dk_ref[...] = jnp.zeros_like(dk_ref)
        dv_ref[...] = jnp.zeros_like(dv_ref)
        lane = lax.broadcasted_iota(jnp.int32, (tq, LANE), 1)
        for hh in range(2):
            mine = jnp.logical_and(lane >= hh * NOPE, lane < (hh + 1) * NOPE)

            def q_step(qi, _, hh=hh, mine=mine):
                q0 = pl.multiple_of(qi * tq, tq)
                qh = q_ref[pl.ds(q0, tq), hh * LANE:(hh + 1) * LANE]
                dof = jnp.where(mine, do_ref[pl.ds(q0, tq), :], 0.0)
                dob = dof.astype(MX)
                delta = jnp.sum(dof * o_ref[pl.ds(q0, tq), :].astype(F32), axis=-1, keepdims=True)
                lse_h = jnp.sum(jnp.where(lane == hh * NOPE, lse_ref[pl.ds(q0, tq), :], 0.0), axis=-1, keepdims=True)

                def k_step(kj, dq_acc):
                    k0 = pl.multiple_of(kj * tk, tk)
                    kb = k_ref[pl.ds(k0, tk), hh * LANE:(hh + 1) * LANE]
                    vb = v_ref[pl.ds(k0, tk), :]
                    s = lax.dot_general(qh, kb, (((1,), (1,)), ((), ())), preferred_element_type=F32) * ATTN_SCALE
                    p = jnp.where(_causal_mask(qi, kj, tq, tk), jnp.exp(s - lse_h), 0.0)
                    dp = lax.dot_general(dob, vb, (((1,), (1,)), ((), ())), preferred_element_type=F32)
                    ds = (p * (dp - delta) * ATTN_SCALE).astype(MX)
                    pb = p.astype(MX)
                    dk_ref[pl.ds(k0, tk), hh * LANE:(hh + 1) * LANE] += lax.dot_general(
                        ds, qh, (((0,), (0,)), ((), ())), preferred_element_type=F32)
                    dv_ref[pl.ds(k0, tk), :] += lax.dot_general(pb, dob, (((0,), (0,)), ((), ())), preferred_element_type=F32)
                    return dq_acc + jnp.dot(ds, kb, preferred_element_type=F32)

                dq_ref[pl.ds(q0, tq), hh * LANE:(hh + 1) * LANE] = lax.fori_loop(0, qi + 1, k_step, jnp.zeros((tq, LANE), F32))
                return 0

            lax.fori_loop(0, nq, q_step, 0)

    pair256 = pl.BlockSpec((S, 2 * LANE), lambda b, p: (b, p))
    pair128 = pl.BlockSpec((S, LANE), lambda b, p: (b, p))
    return _pcall(body, name=name, grid=(nb, npair),
                  in_specs=[pair256, pair256, pair128, pair128, pair128, pair128],
                  out_specs=[pair256, pair256, pair128],
                  out_shape=[_sds((T, NH * LANE), F32), _sds((T, NH * LANE), F32), _sds((T, NH * NOPE), F32)],
                  )(q, k, v, o, lse, do)


def _layer_fwd_bwd(x, tgt, mod3, cos, sin, W, P):
    nb = mod3.shape[0]
    h1 = _fwd_block(x, None, mod3, P["norm_ffn1"], sub=0, coef=0.0, name="fwd_norm1")
    gu1, a1 = _ffn_up(h1, W["ffn1_in"], name="ffn1_up")
    f1 = _mm(a1, W["ffn1_out"], mode="nn", out_dtype=F32, name="ffn1_out", tk=DFF)
    x1, h2 = _fwd_block(x, f1, mod3, P["norm_mix"], sub=1, coef=0.5, name="fwd_norm2")
    z = _mm(h2, W["w_in"], mode="nn", out_dtype=F32, name="mix_in", tn=1664)
    pooled, mixed, scaled = _pool_fwd(z, W["pool_grp"], P["pool_scale"], nb=nb, name="pool_fwd")
    br_pool = _mm(scaled, W["pool_proj"], mode="nn", out_dtype=F32, name="pool_proj")
    qn, kvn = _lat_fwd(z, P["q_a_norm"], P["kv_a_norm"], name="lat_fwd")
    qp = _mm(qn, W["q_up"], mode="nn", out_dtype=F32, name="q_up")
    kvp = _mm(kvn, W["kv_up"], mode="nn", out_dtype=F32, name="kv_up")
    q, k, v = _prep_fwd(qp, kvp, z, cos, sin, P["q_gain"], P["k_gain"], name="prep_fwd")
    attn, lse = _attn_fwd(q, k, v, nb=nb, name="attn_fwd")
    br_mla = _mm(attn, W["mla_proj"], mode="nn", out_dtype=F32, name="mla_proj")
    merged = _merge_fwd(z, br_pool, br_mla, name="merge_fwd")
    mo = _mm(merged, W["w_out"], mode="nn", out_dtype=F32, name="mix_out")
    x2, h3 = _fwd_block(x1, mo, mod3, P["norm_ffn2"], sub=2, coef=1.0, name="fwd_norm3")
    gu2, a2 = _ffn_up(h3, W["ffn2_in"], name="ffn2_up")
    f2 = _mm(a2, W["ffn2_out"], mode="nn", out_dtype=F32, name="ffn2_out", tk=DFF)
    dy, df2, st_fin, loss = _final(x2, f2, tgt, mod3, name="loss_head")

    G = {}
    tokens = x.shape[0]
    G["ffn2_out"] = _mm(a2, df2, mode="tn", out_dtype=F32, name="d_ffn2_out", tm=256, tn=D, tk=tokens, n_outer=True)
    dgu2 = _ffn_dact(df2, gu2, W["ffn2_out"], name="ffn2_dact")
    dh3 = _ffn_dh(dgu2, W["ffn2_in"], name="d_ffn2_h")
    G["ffn2_in"] = _ffn_dw_in(h3, dgu2, name="d_ffn2_in")
    dx2, dmo, st3 = _bwd_block(x2, dh3, dy, mo, mod3, P["norm_ffn2"], sub=2, coef=1.0, name="bwd_norm3")
    G["w_out"] = _mm(merged, dmo, mode="tn", out_dtype=F32, name="d_mix_out")
    dmerged = _mm(dmo, W["w_out"], mode="nt", out_dtype=F32, name="d_merged")
    dbr_pool, dbr_mla, dgates = _merge_bwd(z, br_pool, br_mla, dmerged, name="merge_bwd")
    G["pool_proj"] = _mm(scaled, dbr_pool, mode="tn", out_dtype=F32, name="d_pool_proj")
    dscaled = _mm(dbr_pool, W["pool_proj"], mode="nt", out_dtype=F32, name="d_pool_scaled")
    du_pool, d_pool_grp, d_pool_scale = _pool_bwd(dscaled, mixed, pooled, W["pool_grp"], P["pool_scale"], nb=nb, name="pool_bwd")
    G["mla_proj"] = _mm(attn, dbr_mla, mode="tn", out_dtype=F32, name="d_mla_proj")
    dattn = _mm(dbr_mla, W["mla_proj"], mode="nt", out_dtype=F32, name="d_attn")
    dq, dk, dv = _attn_bwd(q, k, v, attn, lse, dattn, nb=nb, name="attn_bwd")
    dqp, dkvp, dkr, st_prep = _prep_bwd(dq, dk, dv, qp, kvp, z, cos, sin, P["q_gain"], P["k_gain"], nb=nb, name="prep_bwd")
    G["q_up"] = _mm(qn, dqp, mode="tn", out_dtype=F32, name="d_q_up")
    dqn = _mm(dqp, W["q_up"], mode="nt", out_dtype=F32, name="d_qn")
    G["kv_up"] = _mm(kvn, dkvp, mode="tn", out_dtype=F32, name="d_kv_up")
    dkvn = _mm(dkvp, W["kv_up"], mode="nt", out_dtype=F32, name="d_kvn", tk=1536)
    dql, dkvl, st_q, st_kv = _lat_bwd(z, dqn, dkvn, P["q_a_norm"], P["kv_a_norm"], nb=nb, name="lat_bwd")
    dz = jnp.concatenate([du_pool, dql, dkvl, dkr, dgates], axis=1)
    G["w_in"] = _mm(h2, dz, mode="tn", out_dtype=F32, name="d_mix_in", tn=1664)
    dh2 = _mm(dz, W["w_in"], mode="nt", out_dtype=F32, name="d_mix_h", tk=1664)
    dx1, df1, st2 = _bwd_block(x1, dh2, dx2, f1, mod3, P["norm_mix"], sub=1, coef=0.5, name="bwd_norm2")
    G["ffn1_out"] = _mm(a1, df1, mode="tn", out_dtype=F32, name="d_ffn1_out", tm=256, tn=D, tk=tokens, n_outer=True)
    dgu1 = _ffn_dact(df1, gu1, W["ffn1_out"], name="ffn1_dact")
    dh1 = _ffn_dh(dgu1, W["ffn1_in"], name="d_ffn1_h")
    G["ffn1_in"] = _ffn_dw_in(h1, dgu1, name="d_ffn1_in")
    grad_x, st1 = _bwd_block(x, dh1, dx1, None, mod3, P["norm_ffn1"], sub=0, coef=0.0, name="bwd_norm1")

    dmod = jnp.stack([st1[:, 0], st1[:, 1], st2[:, 3], st2[:, 0], st2[:, 1], st3[:, 3], st3[:, 0], st3[:, 1], st_fin[:, 0]],
                     axis=1)
    small = {
        "norm_ffn1": st1[:, 2], "norm_mix": st2[:, 2], "norm_ffn2": st3[:, 2],
        "pool_grp": d_pool_grp, "pool_scale": d_pool_scale[:, 0],
        "q_a_norm": st_q[:, 0], "kv_a_norm": st_kv[:, 0],
        "q_gain": st_prep[:, 0], "k_gain": st_prep[:, 1],
    }
    return loss, grad_x, G, small, dmod


def _w_in_to_kernel(w):
    k = w.shape[0]
    zeros = lambda n: jnp.zeros((k, n), w.dtype)
    return jnp.concatenate([w[:, 0:1152], zeros(KR_LANE), w[:, 1152:1184], zeros(LANE - KR_LANE - ROPE), w[:, 1184:]], axis=1)


def _w_in_from_kernel(g):
    return jnp.concatenate([g[:, 0:1152], g[:, Z_KR + KR_LANE:Z_KR + KR_LANE + ROPE], g[:, Z_GP:]], axis=1)


def _q_up_to_kernel(w):
    k = w.shape[0]
    return jnp.pad(w.reshape(k, NH, NOPE + ROPE), ((0, 0), (0, 0), (0, LANE - NOPE - ROPE))).reshape(k, NH * LANE)


def _q_up_from_kernel(g):
    k = g.shape[0]
    return g.reshape(k, NH, LANE)[:, :, :NOPE + ROPE].reshape(k, NH * (NOPE + ROPE))


def _kv_up_to_kernel(w):
    k = w.shape[0]
    w3 = w.reshape(k, NH, 2 * NOPE)
    kpart = jnp.pad(w3[:, :, :NOPE], ((0, 0), (0, 0), (0, LANE - NOPE))).reshape(k, NH * LANE)
    return jnp.concatenate([kpart, w3[:, :, NOPE:].reshape(k, NH * NOPE)], axis=1)


def _kv_up_from_kernel(g):
    k = g.shape[0]
    kpart = g[:, :NH * LANE].reshape(k, NH, LANE)[:, :, :NOPE]
    vpart = g[:, NH * LANE:].reshape(k, NH, NOPE)
    return jnp.concatenate([kpart, vpart], axis=2).reshape(k, NH * 2 * NOPE)


def _gain_slab(nope, rope):
    return jnp.concatenate([nope, rope, jnp.zeros((1, LANE - NOPE - ROPE), nope.dtype)], axis=1)


def _rope_tables(positions):
    inv_freq = 10000.0 ** (-jnp.arange(0, ROPE, 2, dtype=F32) / ROPE)
    ang = positions.astype(F32)[:, None] * inv_freq
    ang = jnp.concatenate([ang, ang], axis=-1)
    t = positions.shape[0]
    cos = jnp.concatenate([jnp.ones((t, KR_LANE), F32), jnp.cos(ang), jnp.ones((t, LANE - KR_LANE - ROPE), F32)], axis=1)
    sin = jnp.concatenate([jnp.zeros((t, KR_LANE), F32), jnp.sin(ang), jnp.zeros((t, LANE - KR_LANE - ROPE), F32)], axis=1)
    return cos, sin


def _coords():
    return lax.axis_index("x"), lax.axis_index("y"), lax.axis_index("c")


HBM_SPEC = pl.BlockSpec(memory_space=pl.ANY)
VMEM_SPEC = pl.BlockSpec(memory_space=pltpu.VMEM)


def _gather8(v, *, name):
    r, c = v.shape

    def body(v_ref, out_ref, send_sems, recv_sems, local_sem):
        x, y, cc = _coords()
        me = 4 * x + 2 * y + cc
        mine = pltpu.make_async_copy(v_ref, out_ref.at[me], local_sem)
        mine.start()
        copies = []
        for kk in range(1, N_DEV):
            peer = (x ^ (kk >> 2), y ^ ((kk >> 1) & 1), cc ^ (kk & 1))
            cp = pltpu.make_async_remote_copy(src_ref=v_ref, dst_ref=out_ref.at[me], send_sem=send_sems.at[kk - 1],
                                              recv_sem=recv_sems.at[kk - 1], device_id=peer, device_id_type=MESH)
            cp.start()
            copies.append(cp)
        for kk in range(1, N_DEV):
            peer_slot = me ^ kk
            pltpu.make_async_remote_copy(src_ref=v_ref, dst_ref=out_ref.at[peer_slot], send_sem=send_sems.at[kk - 1],
                                         recv_sem=recv_sems.at[kk - 1], device_id=(x, y, cc), device_id_type=MESH).wait_recv()
        for cp in copies:
            cp.wait_send()
        mine.wait()

    return _pcall(body, name=name, out_shape=_sds((N_DEV, r, c), v.dtype), in_specs=[VMEM_SPEC], out_specs=VMEM_SPEC,
                  scratch=[pltpu.SemaphoreType.DMA((N_DEV - 1,)), pltpu.SemaphoreType.DMA((N_DEV - 1,)), pltpu.SemaphoreType.DMA])(v)


CHIP_RELS = ((1, 0), (0, 1), (1, 1))


def _gather_weights(shards, *, name):
    n = len(shards)

    def body(*refs):
        ins, outs = refs[:n], refs[n:2 * n]
        isend, irecv, dsend, drecv, lsem = refs[2 * n:]
        x, y, cc = _coords()
        chip = 2 * x + y
        sib = (x, y, 1 - cc)
        local_copies, sends = [], []
        for a in range(n):
            half = ins[a].shape[0] // 2
            mine = pl.ds(cc * half, half)
            loc = pltpu.make_async_copy(ins[a], outs[a].at[chip], lsem.at[a])
            loc.start()
            local_copies.append(loc)
            for j, (dx, dy) in enumerate(CHIP_RELS):
                cp = pltpu.make_async_remote_copy(src_ref=ins[a].at[mine], dst_ref=outs[a].at[chip, mine],
                                                  send_sem=isend.at[a, j], recv_sem=irecv.at[a, j],
                                                  device_id=(x ^ dx, y ^ dy, cc), device_id_type=MESH)
                cp.start()
                sends.append(cp)
        for a in range(n):
            half = ins[a].shape[0] // 2
            mine = pl.ds(cc * half, half)
            for j, (dx, dy) in enumerate(CHIP_RELS):
                src_chip = 2 * (x ^ dx) + (y ^ dy)
                landed = outs[a].at[src_chip, mine]
                pltpu.make_async_remote_copy(src_ref=landed, dst_ref=landed, send_sem=isend.at[a, j], recv_sem=irecv.at[a, j],
                                             device_id=sib, device_id_type=MESH).wait_recv()
                fwd = pltpu.make_async_remote_copy(src_ref=landed, dst_ref=landed, send_sem=dsend.at[a, j],
                                                   recv_sem=drecv.at[a, j], device_id=sib, device_id_type=MESH)
                fwd.start()
                sends.append(fwd)
        for a in range(n):
            half = ins[a].shape[0] // 2
            theirs = pl.ds((1 - cc) * half, half)
            for j, (dx, dy) in enumerate(CHIP_RELS):
                src_chip = 2 * (x ^ dx) + (y ^ dy)
                landed = outs[a].at[src_chip, theirs]
                pltpu.make_async_remote_copy(src_ref=landed, dst_ref=landed, send_sem=dsend.at[a, j], recv_sem=drecv.at[a, j],
                                             device_id=sib, device_id_type=MESH).wait_recv()
        for cp in sends:
            cp.wait_send()
        for loc in local_copies:
            loc.wait()

    return _pcall(body, name=name, out_shape=[_sds((N_CHIP,) + s.shape, s.dtype) for s in shards],
                  in_specs=[HBM_SPEC] * n, out_specs=[HBM_SPEC] * n,
                  scratch=[pltpu.SemaphoreType.DMA((n, 3)), pltpu.SemaphoreType.DMA((n, 3)), pltpu.SemaphoreType.DMA((n, 3)),
                           pltpu.SemaphoreType.DMA((n, 3)), pltpu.SemaphoreType.DMA((n,))])(*shards)


def _swap_halves(parts, *, name):
    n = len(parts)

    def body(*refs):
        ins, outs = refs[:n], refs[n:2 * n]
        ssem, rsem = refs[2 * n:]
        x, y, cc = _coords()
        sib = (x, y, 1 - cc)
        cps = []
        for a in range(n):
            half = ins[a].shape[1] // 2
            cp = pltpu.make_async_remote_copy(src_ref=ins[a].at[:, pl.ds((1 - cc) * half, half)], dst_ref=outs[a],
                                              send_sem=ssem.at[a], recv_sem=rsem.at[a], device_id=sib, device_id_type=MESH)
            cp.start()
            cps.append(cp)
        for cp in cps:
            cp.wait()

    return _pcall(body, name=name,
                  out_shape=[_sds((p.shape[0], p.shape[1] // 2, p.shape[2]), p.dtype) for p in parts],
                  in_specs=[HBM_SPEC] * n, out_specs=[HBM_SPEC] * n,
                  scratch=[pltpu.SemaphoreType.DMA((n,)), pltpu.SemaphoreType.DMA((n,))])(*parts)


def _add_half(full, other, cidx, *, name):
    nch, r, c = full.shape
    half = r // 2
    tr = _pick_rows(half)
    nbk = half // tr
    grid_spec = pltpu.PrefetchScalarGridSpec(
        num_scalar_prefetch=1, grid=(nch, nbk),
        in_specs=[pl.BlockSpec((1, tr, c), lambda j, i, cref: (j, cref[0] * nbk + i, 0)),
                  pl.BlockSpec((1, tr, c), lambda j, i, cref: (j, i, 0))],
        out_specs=pl.BlockSpec((1, tr, c), lambda j, i, cref: (j, i, 0)))

    def body(cref, a_ref, b_ref, o_ref):
        o_ref[...] = (a_ref[...] + b_ref[...]).astype(o_ref.dtype)

    return _pcall(body, name=name, out_shape=_sds((nch, half, c), MX), grid_spec=grid_spec)(cidx, full, other)


def _pick_rows(rows, target=512):
    best = None
    for t in range(16, min(rows, target) + 1, 16):
        if rows % t == 0:
            best = t
    return rows if best is None else best


def _exchange_chips(parts, *, name):
    n = len(parts)

    def body(*refs):
        ins, outs = refs[:n], refs[n:2 * n]
        ssem, rsem, lsem = refs[2 * n:]
        x, y, cc = _coords()
        chip = 2 * x + y
        cps = []
        for a in range(n):
            loc = pltpu.make_async_copy(ins[a].at[chip], outs[a].at[chip], lsem.at[a])
            loc.start()
            cps.append((loc, None))
            for j, (dx, dy) in enumerate(CHIP_RELS):
                peer_chip = 2 * (x ^ dx) + (y ^ dy)
                cp = pltpu.make_async_remote_copy(src_ref=ins[a].at[peer_chip], dst_ref=outs[a].at[chip],
                                                  send_sem=ssem.at[a, j], recv_sem=rsem.at[a, j],
                                                  device_id=(x ^ dx, y ^ dy, cc), device_id_type=MESH)
                cp.start()
                cps.append((cp, peer_chip))
        for a in range(n):
            for j, (dx, dy) in enumerate(CHIP_RELS):
                peer_chip = 2 * (x ^ dx) + (y ^ dy)
                pltpu.make_async_remote_copy(src_ref=ins[a].at[peer_chip], dst_ref=outs[a].at[peer_chip], send_sem=ssem.at[a, j],
                                             recv_sem=rsem.at[a, j], device_id=(x, y, cc), device_id_type=MESH).wait_recv()
        for cp, peer_chip in cps:
            if peer_chip is None:
                cp.wait()
            else:
                cp.wait_send()

    return _pcall(body, name=name, out_shape=[_sds(p.shape, p.dtype) for p in parts],
                  in_specs=[HBM_SPEC] * n, out_specs=[HBM_SPEC] * n,
                  scratch=[pltpu.SemaphoreType.DMA((n, 3)), pltpu.SemaphoreType.DMA((n, 3)), pltpu.SemaphoreType.DMA((n,))])(*parts)


def _sum_chips(q, cidx, *, name):
    nch, h, c = q.shape
    tr = _pick_rows(h)
    nbk = h // tr
    grid_spec = pltpu.PrefetchScalarGridSpec(
        num_scalar_prefetch=1, grid=(nbk,),
        in_specs=[pl.BlockSpec((nch, tr, c), lambda i, cref: (0, i, 0))],
        out_specs=pl.BlockSpec((tr, c), lambda i, cref: (cref[0] * nbk + i, 0)))

    def body(cref, q_ref, o_ref):
        acc = q_ref[0].astype(F32) + q_ref[1].astype(F32)
        acc = acc + q_ref[2].astype(F32)
        o_ref[...] = acc + q_ref[3].astype(F32)

    return _pcall(body, name=name, out_shape=_sds((2 * h, c), F32), grid_spec=grid_spec)(cidx, q)


def _join_halves(fulls, *, name):
    n = len(fulls)

    def body(*refs):
        outs = refs[n:2 * n]
        ssem, rsem = refs[2 * n:]
        x, y, cc = _coords()
        sib = (x, y, 1 - cc)
        cps = []
        for a in range(n):
            h = outs[a].shape[0] // 2
            mine = outs[a].at[pl.ds(cc * h, h)]
            cp = pltpu.make_async_remote_copy(src_ref=mine, dst_ref=mine, send_sem=ssem.at[a], recv_sem=rsem.at[a],
                                              device_id=sib, device_id_type=MESH)
            cp.start()
            cps.append(cp)
        for a in range(n):
            h = outs[a].shape[0] // 2
            theirs = outs[a].at[pl.ds((1 - cc) * h, h)]
            pltpu.make_async_remote_copy(src_ref=theirs, dst_ref=theirs, send_sem=ssem.at[a], recv_sem=rsem.at[a],
                                         device_id=sib, device_id_type=MESH).wait_recv()
        for cp in cps:
            cp.wait_send()

    return _pcall(body, name=name, out_shape=[_sds(p.shape, p.dtype) for p in fulls],
                  in_specs=[HBM_SPEC] * n, out_specs=[HBM_SPEC] * n, aliases={a: a for a in range(n)},
                  scratch=[pltpu.SemaphoreType.DMA((n,)), pltpu.SemaphoreType.DMA((n,))])(*fulls)


def _ada_fwd(c_all, w, b, *, name):
    m, kdim = c_all.shape
    n = w.shape[1]
    tn = _pick(n, 1152)

    def body(c_ref, w_ref, b_ref, o_ref):
        cv = c_ref[...]
        act = (cv * _sigmoid(cv)).astype(MX)
        o_ref[...] = jnp.dot(act, w_ref[...].astype(MX), preferred_element_type=F32) + b_ref[...]

    return _pcall(body, name=name, out_shape=_sds((m, n), F32), grid=(n // tn,),
                  in_specs=[pl.BlockSpec((m, kdim), lambda j: (0, 0)), pl.BlockSpec((kdim, tn), lambda j: (0, j)),
                            pl.BlockSpec((1, tn), lambda j: (0, j))],
                  out_specs=pl.BlockSpec((m, tn), lambda j: (0, j)))(c_all, w, b)


def _ada_bwd(c_all, dmod_cols, *, name):
    m, kdim = c_all.shape
    n = dmod_cols.shape[1]
    tn = _pick(n, 1152)

    def body(c_ref, d_ref, o_ref):
        cv = c_ref[...]
        act = (cv * _sigmoid(cv)).astype(MX)
        o_ref[...] = lax.dot_general(act, d_ref[...].astype(MX), (((0,), (0,)), ((), ())), preferred_element_type=F32)

    return _pcall(body, name=name, out_shape=_sds((kdim, n), F32), grid=(n // tn,),
                  in_specs=[pl.BlockSpec((m, kdim), lambda j: (0, 0)), pl.BlockSpec((m, tn), lambda j: (0, j))],
                  out_specs=pl.BlockSpec((kdim, tn), lambda j: (0, j)))(c_all, dmod_cols)


def _sum_small(v, nseq, rows, *, name):
    n, r, c = v.shape
    extra = r - nseq * rows

    def body(v_ref, o_ref):
        acc = None
        for d in range(n):
            for s in range(nseq):
                t = v_ref[d, s * rows:(s + 1) * rows, :]
                acc = t if acc is None else acc + t
        o_ref[0:rows, :] = acc
        acc = v_ref[0, nseq * rows:, :]
        for d in range(1, n):
            acc = acc + v_ref[d, nseq * rows:, :]
        o_ref[rows:, :] = acc

    return _pcall(body, name=name, out_shape=_sds((rows + extra, c), F32), in_specs=[VMEM_SPEC], out_specs=VMEM_SPEC)(v)


def _adamw(w, g, m, v, *, name):
    r, c = w.shape
    tr = _pick_rows(r, 256)
    spec = pl.BlockSpec((tr, c), lambda i: (i, 0))
    bc1 = 1.0 / (1.0 - ADAM_B1 ** ADAM_STEP)
    bc2 = 1.0 / (1.0 - ADAM_B2 ** ADAM_STEP)

    def body(w_ref, g_ref, m_ref, v_ref, d_ref, mo_ref, vo_ref):
        gv = g_ref[...]
        mn = ADAM_B1 * m_ref[...] + (1.0 - ADAM_B1) * gv
        vn = ADAM_B2 * v_ref[...] + (1.0 - ADAM_B2) * (gv * gv)
        mo_ref[...] = mn
        vo_ref[...] = vn
        d_ref[...] = -ADAM_LR * ((mn * bc1) / (jnp.sqrt(vn * bc2) + ADAM_EPS) + ADAM_WD * w_ref[...])

    out = _sds((r, c), F32)
    return _pcall(body, name=name, out_shape=[out, out, out], grid=(r // tr,), in_specs=[spec] * 4, out_specs=[spec] * 3)(w, g, m, v)


BIG = ("w_ffn1_in", "w_ffn1_out", "w_in", "w_pool_proj", "w_q_up", "w_kv_up", "w_mla_proj", "w_out", "w_ffn2_in", "w_ffn2_out")
ROW_SHARDED = ("w_ffn1_out", "w_out", "w_ffn2_out")
KERNEL_NAME = {"w_ffn1_in": "ffn1_in", "w_ffn1_out": "ffn1_out", "w_in": "w_in", "w_pool_proj": "pool_proj", "w_q_up": "q_up",
               "w_kv_up": "kv_up", "w_mla_proj": "mla_proj", "w_out": "w_out", "w_ffn2_in": "ffn2_in", "w_ffn2_out": "ffn2_out"}
WEIGHTS = ("w_ada", "b_ada", "norm_ffn1", "w_ffn1_in", "w_ffn1_out", "norm_mix", "w_in", "pool_grp", "pool_scale", "w_pool_proj",
           "q_a_norm", "w_q_up", "kv_a_norm", "w_kv_up", "q_norm_nope", "q_norm_rope", "k_norm_nope", "k_norm_rope", "w_mla_proj",
           "w_out", "norm_ffn2", "w_ffn2_in", "w_ffn2_out")
SMALL = ("b_ada", "norm_ffn1", "norm_mix", "pool_grp", "pool_scale", "q_a_norm", "kv_a_norm", "q_norm_nope", "q_norm_rope",
         "k_norm_nope", "k_norm_rope", "norm_ffn2")
SMALL_SEQ = tuple(n for n in SMALL if n != "pool_grp")
SLAB_W = 1024


def _assemble(name, stacked):
    if name in ROW_SHARDED:
        return stacked.reshape(stacked.shape[0] * stacked.shape[1], stacked.shape[2])
    return jnp.transpose(stacked, (1, 0, 2)).reshape(stacked.shape[1], stacked.shape[0] * stacked.shape[2])


def _split(name, full):
    if name in ROW_SHARDED:
        return full.reshape(N_CHIP, full.shape[0] // N_CHIP, full.shape[1])
    return jnp.transpose(full.reshape(full.shape[0], N_CHIP, full.shape[1] // N_CHIP), (1, 0, 2))


GU = ("w_ffn1_in", "w_ffn2_in")


def _to_rows(v, width=SLAB_W):
    flat = v.reshape(-1)
    rows = -(-flat.shape[0] // width)
    return jnp.pad(flat, (0, rows * width - flat.shape[0])).reshape(rows, width)


def _pack_small(parts, names=SMALL):
    rows, spans, at = [], {}, 0
    for name in names:
        r = _to_rows(parts[name])
        spans[name] = (at, parts[name].size, parts[name].shape)
        rows.append(r)
        at += r.shape[0]
    pad = (-at) % SUBLANE
    if pad:
        rows.append(jnp.zeros((pad, SLAB_W), F32))
    return jnp.concatenate(rows, axis=0), spans


def _unpack_small(slab, spans):
    out = {}
    for name, (at, size, shape) in spans.items():
        nrow = -(-size // SLAB_W)
        out[name] = slab[at:at + nrow].reshape(-1)[:size].reshape(shape)
    return out


def kernel(x, c, positions, w_ada, b_ada, norm_ffn1, w_ffn1_in, w_ffn1_out, norm_mix, w_in, pool_grp, pool_scale, w_pool_proj, q_a_norm, w_q_up, kv_a_norm, w_kv_up, q_norm_nope, q_norm_rope, k_norm_nope, k_norm_rope, w_mla_proj, w_out, norm_ffn2, w_ffn2_in, w_ffn2_out, loss_target, m_w_ada, m_b_ada, m_norm_ffn1, m_w_ffn1_in, m_w_ffn1_out, m_norm_mix, m_w_in, m_pool_grp, m_pool_scale, m_w_pool_proj, m_q_a_norm, m_w_q_up, m_kv_a_norm, m_w_kv_up, m_q_norm_nope, m_q_norm_rope, m_k_norm_nope, m_k_norm_rope, m_w_mla_proj, m_w_out, m_norm_ffn2, m_w_ffn2_in, m_w_ffn2_out, v_w_ada, v_b_ada, v_norm_ffn1, v_w_ffn1_in, v_w_ffn1_out, v_norm_mix, v_w_in, v_pool_grp, v_pool_scale, v_w_pool_proj, v_q_a_norm, v_w_q_up, v_kv_a_norm, v_w_kv_up, v_q_norm_nope, v_q_norm_rope, v_k_norm_nope, v_k_norm_rope, v_w_mla_proj, v_w_out, v_norm_ffn2, v_w_ffn2_in, v_w_ffn2_out):
    args = dict(locals())
    wts = {n: args[n][0] for n in WEIGHTS}
    mom = {n: args["m_" + n][0] for n in WEIGHTS}
    var = {n: args["v_" + n][0] for n in WEIGHTS}
    nb, seq, dm = x.shape
    tokens = nb * seq
    xi, yi, ci = _coords()
    chip = 2 * xi + yi
    dev = 4 * xi + 2 * yi + ci

    c_all = _gather8(c, name="gather_c").reshape(N_DEV * nb, dm)
    ncol = w_ada.shape[2]
    b_cols = lax.dynamic_slice_in_dim(wts["b_ada"].reshape(1, -1), chip * ncol, ncol, axis=1)
    mod_cols = _ada_fwd(c_all, wts["w_ada"], b_cols, name="ada_fwd")
    mod_all = _gather8(mod_cols, name="gather_mod")
    mine = lax.dynamic_slice_in_dim(mod_all, dev * nb, nb, axis=1)
    mod = jnp.concatenate([mine[0], mine[2], mine[4], mine[6]], axis=1)
    mod3 = mod.reshape(nb, 9, dm)

    gathered = _gather_weights([wts[n].astype(MX) for n in BIG], name="gather_weights")
    full = {n: _assemble(n, g) for n, g in zip(BIG, gathered) if n not in GU}
    W = {KERNEL_NAME[n]: full[n] for n in BIG if n not in GU}
    for n, g in zip(BIG, gathered):
        if n in GU:
            W[KERNEL_NAME[n]] = g
    W["w_in"] = _w_in_to_kernel(full["w_in"])
    W["q_up"] = _q_up_to_kernel(full["w_q_up"])
    W["kv_up"] = _kv_up_to_kernel(full["w_kv_up"])
    W["pool_grp"] = wts["pool_grp"].astype(MX)
    P = {"norm_ffn1": wts["norm_ffn1"].reshape(1, dm), "norm_mix": wts["norm_mix"].reshape(1, dm),
         "norm_ffn2": wts["norm_ffn2"].reshape(1, dm), "pool_scale": wts["pool_scale"].reshape(1, POOL_W),
         "q_a_norm": wts["q_a_norm"].reshape(1, QL), "kv_a_norm": wts["kv_a_norm"].reshape(1, KVL),
         "q_gain": _gain_slab(wts["q_norm_nope"].reshape(1, NOPE), wts["q_norm_rope"].reshape(1, ROPE)),
         "k_gain": _gain_slab(wts["k_norm_nope"].reshape(1, NOPE), wts["k_norm_rope"].reshape(1, ROPE))}
    cos, sin = _rope_tables(positions.reshape(tokens))

    loss8, grad_x, G, small, dmod = _layer_fwd_bwd(x.reshape(tokens, dm), loss_target.reshape(tokens, dm), mod3, cos, sin, W, P)
    loss = lax.psum(loss8[0, 0], ("x", "y", "c"))

    gfull = {KERNEL_NAME[n]: G[KERNEL_NAME[n]] for n in BIG}
    gfull["w_in"] = _w_in_from_kernel(G["w_in"])
    gfull["q_up"] = _q_up_from_kernel(G["q_up"])
    gfull["kv_up"] = _kv_up_from_kernel(G["kv_up"])
    parts = [G[KERNEL_NAME[n]] if n in GU else _split(n, gfull[KERNEL_NAME[n]]) for n in BIG]
    cidx = ci.astype(jnp.int32).reshape(1)
    from_sib = _swap_halves(parts, name="rs_swap_halves")
    pre = [_add_half(p, o, cidx, name="rs_add_" + KERNEL_NAME[n]) for n, p, o in zip(BIG, parts, from_sib)]
    landed = _exchange_chips(pre, name="rs_exchange_chips")
    halves = [_sum_chips(q, cidx, name="rs_sum_" + KERNEL_NAME[n]) for n, q in zip(BIG, landed)]
    joined = _join_halves(halves, name="rs_join_halves")
    grads = {n: g for n, g in zip(BIG, joined)}

    qg, kg = small["q_gain"], small["k_gain"]
    per_seq = {"b_ada": dmod.reshape(nb, 9 * dm), "norm_ffn1": small["norm_ffn1"], "norm_mix": small["norm_mix"],
               "pool_scale": small["pool_scale"], "q_a_norm": small["q_a_norm"],
               "kv_a_norm": small["kv_a_norm"], "q_norm_nope": qg[:, :NOPE], "q_norm_rope": qg[:, NOPE:NOPE + ROPE],
               "k_norm_nope": kg[:, :NOPE], "k_norm_rope": kg[:, KR_LANE:KR_LANE + ROPE], "norm_ffn2": small["norm_ffn2"]}
    slabs, spans = [], None
    for s in range(nb):
        slab, spans = _pack_small({n: per_seq[n][s] for n in SMALL_SEQ}, SMALL_SEQ)
        slabs.append(slab)
    rows = slabs[0].shape[0]
    slabs.append(_to_rows(small["pool_grp"]))
    gathered_small = _gather8(jnp.concatenate(slabs, axis=0), name="gather_small")
    small_sum = _sum_small(gathered_small, nb, rows, name="sum_small")
    small_grads = _unpack_small(small_sum[:rows], spans)
    small_grads["pool_grp"] = small_sum[rows:]
    for n in SMALL:
        grads[n] = small_grads[n].reshape(wts[n].shape)
    at, _, _ = spans["b_ada"]
    nrow_b = 9 * dm // SLAB_W
    dmod_all = gathered_small[:, :nb * rows].reshape(N_DEV * nb, rows, SLAB_W)[:, at:at + nrow_b].reshape(N_DEV * nb, 9 * dm)
    dmod_cols = lax.dynamic_slice_in_dim(dmod_all, chip * ncol, ncol, axis=1)
    grads["w_ada"] = _ada_bwd(c_all, dmod_cols, name="ada_bwd")

    delta, new_m, new_v = {}, {}, {}
    for n in ("w_ada",) + BIG:
        delta[n], new_m[n], new_v[n] = _adamw(wts[n], grads[n], mom[n], var[n], name="adamw_" + n)
    w_slab, sp = _pack_small({n: wts[n] for n in SMALL})
    g_slab, _ = _pack_small({n: grads[n] for n in SMALL})
    m_slab, _ = _pack_small({n: mom[n] for n in SMALL})
    v_slab, _ = _pack_small({n: var[n] for n in SMALL})
    d_s, m_s, v_s = _adamw(w_slab, g_slab, m_slab, v_slab, name="adamw_small")
    for dst, slab in ((delta, d_s), (new_m, m_s), (new_v, v_s)):
        un = _unpack_small(slab, sp)
        for n in SMALL:
            dst[n] = un[n]

    def lead(a, n):
        return a.reshape((1,) + wts[n].shape)

    return (loss, grad_x.reshape(nb, seq, dm), *[lead(grads[n], n) for n in WEIGHTS], *[lead(delta[n], n) for n in WEIGHTS],
            *[lead(new_m[n], n) for n in WEIGHTS], *[lead(new_v[n], n) for n in WEIGHTS])
```

```python
import functools
import math

import jax
import jax.numpy as jnp
from jax import lax
from jax.experimental import pallas as pl
from jax.experimental.pallas import tpu as pltpu

F32 = jnp.float32
MX = jnp.bfloat16

D = 1024
DFF = 2816
NH = 8
POOL_W = 512
POOL_G = 4
QL = 384
KVL = 256
ROPE = 32
NOPE = 64
LANE = 128
SUBLANE = 8
EPS = 1e-6
ATTN_SCALE = 1.0 / math.sqrt(96.0)
NEG = -1e30

Z_UP, Z_QL, Z_KV, Z_KR, Z_GP, Z_GM, Z_W = 0, 512, 896, 1152, 1280, 2304, 3328
KR_LANE = 64
LAT_W = 1280

ADAM_LR, ADAM_B1, ADAM_B2, ADAM_EPS, ADAM_WD, ADAM_STEP = 0.001, 0.9, 0.999, 1e-08, 0.01, 10

VMEM_LIMIT = 48 * 1024 * 1024
MESH = pl.DeviceIdType.MESH
N_DEV = 8
N_CHIP = 4


def _pcall(body, *, name, out_shape, grid=(), in_specs=None, out_specs=None, scratch=(), grid_spec=None, aliases=None):
    params = pltpu.CompilerParams(vmem_limit_bytes=VMEM_LIMIT)
    kw = dict(name=name, out_shape=out_shape, compiler_params=params)
    if aliases:
        kw["input_output_aliases"] = aliases
    if grid_spec is not None:
        return pl.pallas_call(body, grid_spec=grid_spec, **kw)
    return pl.pallas_call(body, grid=grid, in_specs=in_specs, out_specs=out_specs, scratch_shapes=scratch, **kw)


def _pick(dim, target):
    best = None
    for t in range(LANE, min(dim, target) + 1, LANE):
        if dim % t == 0:
            best = t
    return dim if best is None else best


def _sds(shape, dtype):
    return jax.ShapeDtypeStruct(shape, dtype)


def _mm(a, b, *, mode, out_dtype, name, tm=512, tn=512, tk=1024, n_outer=False):
    if mode == "nn":
        (M, K), (K2, N) = a.shape, b.shape
    elif mode == "nt":
        (M, K), (N, K2) = a.shape, b.shape
    else:
        (K, M), (K2, N) = a.shape, b.shape
    assert K == K2, (name, a.shape, b.shape)
    tm, tn, tk = _pick(M, tm), _pick(N, tn), _pick(K, tk)
    nk = K // tk
    if n_outer:
        ij = lambda g0, g1: (g1, g0)
        grid = (N // tn, M // tm, nk)
    else:
        ij = lambda g0, g1: (g0, g1)
        grid = (M // tm, N // tn, nk)
    if mode == "tn":
        a_spec = pl.BlockSpec((tk, tm), lambda g0, g1, k: (k, ij(g0, g1)[0]))
    else:
        a_spec = pl.BlockSpec((tm, tk), lambda g0, g1, k: (ij(g0, g1)[0], k))
    if mode == "nt":
        b_spec = pl.BlockSpec((tn, tk), lambda g0, g1, k: (ij(g0, g1)[1], k))
    else:
        b_spec = pl.BlockSpec((tk, tn), lambda g0, g1, k: (k, ij(g0, g1)[1]))
    o_spec = pl.BlockSpec((tm, tn), lambda g0, g1, k: ij(g0, g1))
    dn = {"nn": (((1,), (0,)), ((), ())), "nt": (((1,), (1,)), ((), ())), "tn": (((0,), (0,)), ((), ()))}[mode]

    def dot(a_ref, b_ref):
        return lax.dot_general(a_ref[...].astype(MX), b_ref[...].astype(MX), dn, preferred_element_type=F32)

    def body_one(a_ref, b_ref, o_ref):
        o_ref[...] = dot(a_ref, b_ref).astype(o_ref.dtype)

    def body_acc(a_ref, b_ref, o_ref, acc_ref):
        k = pl.program_id(2)
        part = dot(a_ref, b_ref)

        @pl.when(k == 0)
        def _():
            acc_ref[...] = part

        @pl.when(k > 0)
        def _():
            acc_ref[...] += part

        @pl.when(k == nk - 1)
        def _():
            o_ref[...] = acc_ref[...].astype(o_ref.dtype)

    return _pcall(body_one if nk == 1 else body_acc, name=name, out_shape=_sds((M, N), out_dtype), grid=grid,
                  in_specs=[a_spec, b_spec], out_specs=o_spec, scratch=[] if nk == 1 else [pltpu.VMEM((tm, tn), F32)])(a, b)


def _gu_shard(q):
    return (q % 2) * 2 + q // 2


def _ffn_up(h, w_st, *, name, tm=512):
    T, dm = h.shape
    hw = w_st.shape[2]
    tm = _pick(T, tm)

    def body(h_ref, wg_ref, wu_ref, gu_ref, a_ref):
        hv = h_ref[...]
        g = jnp.dot(hv, wg_ref[0], preferred_element_type=F32)
        u = jnp.dot(hv, wu_ref[0], preferred_element_type=F32)
        gu_ref[:, :hw] = g.astype(gu_ref.dtype)
        gu_ref[:, hw:] = u.astype(gu_ref.dtype)
        a_ref[...] = (g * _sigmoid(g) * u).astype(a_ref.dtype)

    return _pcall(body, name=name, grid=(T // tm, 2),
                  in_specs=[pl.BlockSpec((tm, dm), lambda i, j: (i, 0)), pl.BlockSpec((1, dm, hw), lambda i, j: (j, 0, 0)),
                            pl.BlockSpec((1, dm, hw), lambda i, j: (2 + j, 0, 0))],
                  out_specs=[pl.BlockSpec((tm, 2 * hw), lambda i, j: (i, j)), pl.BlockSpec((tm, hw), lambda i, j: (i, j))],
                  out_shape=[_sds((T, 4 * hw), MX), _sds((T, 2 * hw), MX)])(h, w_st, w_st)


def _ffn_dact(df, gu, w_out, *, name, tm=512):
    T, dm = df.shape
    hw = gu.shape[1] // 4
    tm = _pick(T, tm)

    def body(df_ref, gu_ref, wo_ref, dgu_ref):
        da = lax.dot_general(df_ref[...], wo_ref[...], (((1,), (1,)), ((), ())), preferred_element_type=F32)
        g = gu_ref[:, :hw].astype(F32)
        u = gu_ref[:, hw:].astype(F32)
        s = _sigmoid(g)
        dgu_ref[:, :hw] = (da * u * (s * (1.0 + g * (1.0 - s)))).astype(dgu_ref.dtype)
        dgu_ref[:, hw:] = (da * (g * s)).astype(dgu_ref.dtype)

    return _pcall(body, name=name, grid=(T // tm, 2),
                  in_specs=[pl.BlockSpec((tm, dm), lambda i, j: (i, 0)), pl.BlockSpec((tm, 2 * hw), lambda i, j: (i, j)),
                            pl.BlockSpec((hw, dm), lambda i, j: (j, 0))],
                  out_specs=pl.BlockSpec((tm, 2 * hw), lambda i, j: (i, j)), out_shape=_sds(gu.shape, MX))(df, gu, w_out)


def _ffn_dh(dgu, w_st, *, name, tm=1024):
    T = dgu.shape[0]
    _, dm, hw = w_st.shape
    tm = _pick(T, tm)

    def body(d_ref, w_ref, o_ref, acc_ref):
        q = pl.program_id(1)
        part = lax.dot_general(d_ref[...], w_ref[0], (((1,), (1,)), ((), ())), preferred_element_type=F32)

        @pl.when(q == 0)
        def _():
            acc_ref[...] = part

        @pl.when(q > 0)
        def _():
            acc_ref[...] += part

        @pl.when(q == 3)
        def _():
            o_ref[...] = acc_ref[...]

    return _pcall(body, name=name, grid=(T // tm, 4),
                  in_specs=[pl.BlockSpec((tm, hw), lambda i, q: (i, q)), pl.BlockSpec((1, dm, hw), lambda i, q: (_gu_shard(q), 0, 0))],
                  out_specs=pl.BlockSpec((tm, dm), lambda i, q: (i, 0)), out_shape=_sds((T, dm), F32),
                  scratch=[pltpu.VMEM((tm, dm), F32)])(dgu, w_st)


def _ffn_dw_in(h, dgu, *, name, tm=256):
    T, dm = h.shape
    hw = dgu.shape[1] // 4
    tm = _pick(dm, tm)

    def body(h_ref, d_ref, o_ref):
        o_ref[0] = lax.dot_general(h_ref[...], d_ref[...], (((0,), (0,)), ((), ())), preferred_element_type=F32)

    return _pcall(body, name=name, grid=(4, dm // tm),
                  in_specs=[pl.BlockSpec((T, tm), lambda q, i: (0, i)), pl.BlockSpec((T, hw), lambda q, i: (0, q))],
                  out_specs=pl.BlockSpec((1, tm, hw), lambda q, i: (_gu_shard(q), i, 0)),
                  out_shape=_sds((4, dm, hw), F32))(h, dgu)


def _rsq(x):
    return lax.rsqrt(jnp.mean(x * x, axis=-1, keepdims=True) + EPS)


def _sigmoid(x):
    return 1.0 / (1.0 + jnp.exp(-x))


def _row_spec(tm, w):
    return pl.BlockSpec((tm, w), lambda i: (i, 0))


def _fwd_block(x_prev, f_prev, mod3, gain, *, sub, coef, name, tm=256):
    T, dm = x_prev.shape
    tps = (T // mod3.shape[0]) // tm
    has_f = f_prev is not None
    mod_spec = pl.BlockSpec((1, 9, dm), lambda i: (i // tps, 0, 0))
    vec_spec = pl.BlockSpec((1, dm), lambda i: (0, 0))

    def body(*refs):
        if has_f:
            x_ref, f_ref, mod_ref, n_ref, xo_ref, h_ref = refs
            x = x_ref[...] + coef * mod_ref[0, 3 * sub - 1:3 * sub, :] * f_ref[...]
            xo_ref[...] = x
        else:
            x_ref, mod_ref, n_ref, h_ref = refs
            x = x_ref[...]
        xn = x * _rsq(x) * n_ref[...]
        h = xn * (1.0 + mod_ref[0, 3 * sub + 1:3 * sub + 2, :]) + mod_ref[0, 3 * sub:3 * sub + 1, :]
        h_ref[...] = h.astype(h_ref.dtype)

    row = _row_spec(tm, dm)
    if has_f:
        return _pcall(body, name=name, grid=(T // tm,), in_specs=[row, row, mod_spec, vec_spec], out_specs=[row, row],
                      out_shape=[_sds((T, dm), F32), _sds((T, dm), MX)])(x_prev, f_prev, mod3, gain)
    return _pcall(body, name=name, grid=(T // tm,), in_specs=[row, mod_spec, vec_spec], out_specs=row,
                  out_shape=_sds((T, dm), MX))(x_prev, mod3, gain)


def _final(x2, f2, tgt, mod3, *, name, tm=256):
    T, dm = x2.shape
    tps = (T // mod3.shape[0]) // tm
    mod_spec = pl.BlockSpec((1, 9, dm), lambda i: (i // tps, 0, 0))
    row = _row_spec(tm, dm)
    stat_spec = pl.BlockSpec((1, SUBLANE, dm), lambda i: (i // tps, 0, 0))
    loss_spec = pl.BlockSpec((SUBLANE, LANE), lambda i: (0, 0))

    def body(x_ref, f_ref, t_ref, mod_ref, dy_ref, df_ref, st_ref, loss_ref):
        i = pl.program_id(0)
        g = mod_ref[0, 8:9, :]
        f = f_ref[...]
        err = x_ref[...] + 0.5 * g * f - t_ref[...]
        dy = err * (1.0 / dm)
        dy_ref[...] = dy
        df_ref[...] = (0.5 * g * dy).astype(df_ref.dtype)
        dgate = jnp.sum(0.5 * dy * f, axis=0, keepdims=True)
        part = 0.5 * jnp.sum(jnp.sum(err * err, axis=0, keepdims=True), axis=1, keepdims=True) * (1.0 / dm)

        @pl.when(i % tps == 0)
        def _():
            st_ref[...] = jnp.zeros_like(st_ref)

        @pl.when(i == 0)
        def _():
            loss_ref[...] = jnp.zeros_like(loss_ref)

        st_ref[0, 0:1, :] += dgate
        loss_ref[...] += jnp.broadcast_to(part, loss_ref.shape)

    return _pcall(body, name=name, grid=(T // tm,), in_specs=[row, row, row, mod_spec],
                  out_specs=[row, row, stat_spec, loss_spec],
                  out_shape=[_sds((T, dm), F32), _sds((T, dm), MX), _sds((mod3.shape[0], SUBLANE, dm), F32),
                             _sds((SUBLANE, LANE), F32)])(x2, f2, tgt, mod3)


def _bwd_block(x_cur, dh, dx_in, f_prev, mod3, gain, *, sub, coef, name, tm=256):
    T, dm = x_cur.shape
    nb = mod3.shape[0]
    tps = (T // nb) // tm
    has_f = f_prev is not None
    mod_spec = pl.BlockSpec((1, 9, dm), lambda i: (i // tps, 0, 0))
    vec_spec = pl.BlockSpec((1, dm), lambda i: (0, 0))
    stat_spec = pl.BlockSpec((1, SUBLANE, dm), lambda i: (i // tps, 0, 0))
    row = _row_spec(tm, dm)

    def body(*refs):
        if has_f:
            x_ref, dh_ref, dxi_ref, f_ref, mod_ref, n_ref, dx_ref, df_ref, st_ref = refs
        else:
            x_ref, dh_ref, dxi_ref, mod_ref, n_ref, dx_ref, st_ref = refs
        i = pl.program_id(0)
        x = x_ref[...]
        r = _rsq(x)
        xhat = x * r
        n = n_ref[...]
        dhv = dh_ref[...]
        d_shift = jnp.sum(dhv, axis=0, keepdims=True)
        d_scale = jnp.sum(dhv * (xhat * n), axis=0, keepdims=True)
        dxn = dhv * (1.0 + mod_ref[0, 3 * sub + 1:3 * sub + 2, :])
        d_gain = jnp.sum(dxn * xhat, axis=0, keepdims=True)
        dxhat = dxn * n
        dx = dxi_ref[...] + r * (dxhat - xhat * jnp.mean(dxhat * xhat, axis=-1, keepdims=True))
        dx_ref[...] = dx

        @pl.when(i % tps == 0)
        def _():
            st_ref[...] = jnp.zeros_like(st_ref)

        st_ref[0, 0:1, :] += d_shift
        st_ref[0, 1:2, :] += d_scale
        st_ref[0, 2:3, :] += d_gain
        if has_f:
            f = f_ref[...]
            st_ref[0, 3:4, :] += jnp.sum(coef * dx * f, axis=0, keepdims=True)
            df_ref[...] = (coef * mod_ref[0, 3 * sub - 1:3 * sub, :] * dx).astype(df_ref.dtype)

    st_shape = _sds((nb, SUBLANE, dm), F32)
    if has_f:
        return _pcall(body, name=name, grid=(T // tm,), in_specs=[row, row, row, row, mod_spec, vec_spec],
                      out_specs=[row, row, stat_spec],
                      out_shape=[_sds((T, dm), F32), _sds((T, dm), MX), st_shape])(x_cur, dh, dx_in, f_prev, mod3, gain)
    return _pcall(body, name=name, grid=(T // tm,), in_specs=[row, row, row, mod_spec, vec_spec],
                  out_specs=[row, stat_spec], out_shape=[_sds((T, dm), F32), st_shape])(x_cur, dh, dx_in, mod3, gain)


def _merge_fwd(z, br_pool, br_mla, *, name, tm=256):
    T = z.shape[0]

    def body(z_ref, bp_ref, bm_ref, o_ref):
        gp = z_ref[:, Z_GP:Z_GP + D]
        gm = z_ref[:, Z_GM:Z_GM + D]
        o_ref[...] = (_sigmoid(gp) * bp_ref[...] + _sigmoid(gm) * bm_ref[...]).astype(o_ref.dtype)

    return _pcall(body, name=name, grid=(T // tm,), in_specs=[_row_spec(tm, Z_W), _row_spec(tm, D), _row_spec(tm, D)],
                  out_specs=_row_spec(tm, D), out_shape=_sds((T, D), MX))(z, br_pool, br_mla)


def _merge_bwd(z, br_pool, br_mla, dmerged, *, name, tm=256):
    T = z.shape[0]

    def body(z_ref, bp_ref, bm_ref, dm_ref, dbp_ref, dbm_ref, dg_ref):
        dm = dm_ref[...]
        sp = _sigmoid(z_ref[:, Z_GP:Z_GP + D])
        sm = _sigmoid(z_ref[:, Z_GM:Z_GM + D])
        dbp_ref[...] = (dm * sp).astype(dbp_ref.dtype)
        dbm_ref[...] = (dm * sm).astype(dbm_ref.dtype)
        dg_ref[:, :D] = (dm * bp_ref[...] * sp * (1.0 - sp)).astype(dg_ref.dtype)
        dg_ref[:, D:] = (dm * bm_ref[...] * sm * (1.0 - sm)).astype(dg_ref.dtype)

    return _pcall(body, name=name, grid=(T // tm,),
                  in_specs=[_row_spec(tm, Z_W), _row_spec(tm, D), _row_spec(tm, D), _row_spec(tm, D)],
                  out_specs=[_row_spec(tm, D), _row_spec(tm, D), _row_spec(tm, 2 * D)],
                  out_shape=[_sds((T, D), MX), _sds((T, D), MX), _sds((T, 2 * D), MX)])(z, br_pool, br_mla, dmerged)


def _shift_down(x, k, row):
    return jnp.where(row >= k, pltpu.roll(x, k, 0), 0.0)


def _shift_up(x, k, row, n):
    return jnp.where(row < n - k, pltpu.roll(x, n - k, 0), 0.0)


def _pool_fwd(z, pool_grp, pool_scale, *, nb, name):
    T = z.shape[0]
    S = T // nb
    blk = pl.BlockSpec((S, LANE), lambda b, g: (b, g))

    def body(u_ref, w_ref, s_ref, pooled_ref, mixed_ref, scaled_ref):
        g = pl.program_id(1)
        u = u_ref[...]
        row = lax.broadcasted_iota(jnp.int32, u.shape, 0)
        s2 = u + _shift_down(u, 1, row)
        s4 = s2 + _shift_down(s2, 2, row)
        s8 = s4 + _shift_down(s4, 4, row)
        s16 = s8 + _shift_down(s8, 8, row)
        win = jnp.where(g == 0, s2, jnp.where(g == 1, s4, jnp.where(g == 2, s8, s16)))
        width = lax.shift_left(jnp.int32(2), g)
        cnt = jnp.minimum(row + 1, width).astype(F32)
        pooled = (win / cnt - u).astype(MX)
        pooled_ref[...] = pooled
        mixed = jnp.dot(pooled, w_ref[0], preferred_element_type=F32)
        mixed_ref[...] = mixed
        scaled_ref[...] = (mixed * s_ref[...]).astype(scaled_ref.dtype)

    return _pcall(body, name=name, grid=(nb, POOL_G),
                  in_specs=[blk, pl.BlockSpec((1, LANE, LANE), lambda b, g: (g, 0, 0)),
                            pl.BlockSpec((1, LANE), lambda b, g: (0, g))],
                  out_specs=[blk, blk, blk],
                  out_shape=[_sds((T, POOL_W), MX), _sds((T, POOL_W), F32), _sds((T, POOL_W), MX)])(z, pool_grp, pool_scale)


def _pool_bwd(dscaled, mixed, pooled, pool_grp, pool_scale, *, nb, name):
    T = dscaled.shape[0]
    S = T // nb
    blk = pl.BlockSpec((S, LANE), lambda g, b: (b, g))

    def body(ds_ref, mixed_ref, pooled_ref, w_ref, s_ref, du_ref, dw_ref, dsc_ref):
        g = pl.program_id(0)
        b = pl.program_id(1)
        ds = ds_ref[...]
        dsc_ref[0] = jnp.sum(ds * mixed_ref[...], axis=0, keepdims=True)
        dmixed = (ds * s_ref[...]).astype(MX)
        dw = lax.dot_general(pooled_ref[...], dmixed, (((0,), (0,)), ((), ())), preferred_element_type=F32)

        @pl.when(b == 0)
        def _():
            dw_ref[0] = dw

        @pl.when(b > 0)
        def _():
            dw_ref[0] += dw
        dpooled = lax.dot_general(dmixed, w_ref[0], (((1,), (1,)), ((), ())), preferred_element_type=F32)
        row = lax.broadcasted_iota(jnp.int32, dpooled.shape, 0)
        width = lax.shift_left(jnp.int32(2), g)
        q = dpooled / jnp.minimum(row + 1, width).astype(F32)
        r2 = q + _shift_up(q, 1, row, S)
        r4 = r2 + _shift_up(r2, 2, row, S)
        r8 = r4 + _shift_up(r4, 4, row, S)
        r16 = r8 + _shift_up(r8, 8, row, S)
        win = jnp.where(g == 0, r2, jnp.where(g == 1, r4, jnp.where(g == 2, r8, r16)))
        du_ref[...] = (win - dpooled).astype(du_ref.dtype)

    return _pcall(body, name=name, grid=(POOL_G, nb),
                  in_specs=[blk, blk, blk, pl.BlockSpec((1, LANE, LANE), lambda g, b: (g, 0, 0)),
                            pl.BlockSpec((1, LANE), lambda g, b: (0, g))],
                  out_specs=[blk, pl.BlockSpec((1, LANE, LANE), lambda g, b: (g, 0, 0)),
                             pl.BlockSpec((1, 1, LANE), lambda g, b: (b, 0, g))],
                  out_shape=[_sds((T, POOL_W), MX), _sds((POOL_G, LANE, LANE), F32), _sds((nb, 1, POOL_W), F32)],
                  )(dscaled, mixed, pooled, pool_grp, pool_scale)


def _lat_fwd(z, q_gain, kv_gain, *, name, tm=256):
    T = z.shape[0]

    def body(z_ref, qg_ref, kg_ref, qn_ref, kvn_ref):
        ql = z_ref[:, Z_QL:Z_QL + QL]
        kv = z_ref[:, Z_KV:Z_KV + KVL]
        qn_ref[...] = (ql * _rsq(ql) * qg_ref[...]).astype(qn_ref.dtype)
        kvn_ref[...] = (kv * _rsq(kv) * kg_ref[...]).astype(kvn_ref.dtype)

    return _pcall(body, name=name, grid=(T // tm,),
                  in_specs=[_row_spec(tm, LAT_W), pl.BlockSpec((1, QL), lambda i: (0, 0)), pl.BlockSpec((1, KVL), lambda i: (0, 0))],
                  out_specs=[_row_spec(tm, QL), _row_spec(tm, KVL)],
                  out_shape=[_sds((T, QL), MX), _sds((T, KVL), MX)])(z, q_gain, kv_gain)


def _lat_bwd(z, dqn, dkvn, q_gain, kv_gain, *, nb, name, tm=256):
    T = z.shape[0]
    tps = (T // nb) // tm

    def norm_bwd(x, dy, gain):
        r = _rsq(x)
        xhat = x * r
        dgain = jnp.sum(dy * xhat, axis=0, keepdims=True)
        dxhat = dy * gain
        return r * (dxhat - xhat * jnp.mean(dxhat * xhat, axis=-1, keepdims=True)), dgain

    def body(z_ref, dq_ref, dkv_ref, qg_ref, kg_ref, dql_ref, dkvl_ref, sq_ref, sk_ref):
        i = pl.program_id(0)
        dql, dqg = norm_bwd(z_ref[:, Z_QL:Z_QL + QL], dq_ref[...], qg_ref[...])
        dkvl, dkg = norm_bwd(z_ref[:, Z_KV:Z_KV + KVL], dkv_ref[...], kg_ref[...])
        dql_ref[...] = dql.astype(dql_ref.dtype)
        dkvl_ref[...] = dkvl.astype(dkvl_ref.dtype)

        @pl.when(i % tps == 0)
        def _():
            sq_ref[...] = jnp.zeros_like(sq_ref)
            sk_ref[...] = jnp.zeros_like(sk_ref)

        sq_ref[0, 0:1, :] += dqg
        sk_ref[0, 0:1, :] += dkg

    return _pcall(body, name=name, grid=(T // tm,),
                  in_specs=[_row_spec(tm, LAT_W), _row_spec(tm, QL), _row_spec(tm, KVL),
                            pl.BlockSpec((1, QL), lambda i: (0, 0)), pl.BlockSpec((1, KVL), lambda i: (0, 0))],
                  out_specs=[_row_spec(tm, QL), _row_spec(tm, KVL),
                             pl.BlockSpec((1, SUBLANE, QL), lambda i: (i // tps, 0, 0)),
                             pl.BlockSpec((1, SUBLANE, KVL), lambda i: (i // tps, 0, 0))],
                  out_shape=[_sds((T, QL), MX), _sds((T, KVL), MX), _sds((nb, SUBLANE, QL), F32), _sds((nb, SUBLANE, KVL), F32)],
                  )(z, dqn, dkvn, q_gain, kv_gain)


def _lane_masks(shape):
    lane = lax.broadcasted_iota(jnp.int32, shape, len(shape) - 1)
    m_n = lane < NOPE
    m_r = jnp.logical_and(lane >= KR_LANE, lane < KR_LANE + ROPE)
    first_half = lane < KR_LANE + ROPE // 2
    return m_n, m_r, first_half


def _rot(y, first_half):
    return jnp.where(first_half, -pltpu.roll(y, LANE - ROPE // 2, 1), pltpu.roll(y, ROPE // 2, 1))


def _rot_t(v, first_half, m_r):
    return jnp.where(m_r, jnp.where(first_half, pltpu.roll(v, LANE - ROPE // 2, 1), -pltpu.roll(v, ROPE // 2, 1)), 0.0)


def _prep_fwd(qp, kvp, z, cos, sin, q_gain, k_gain, *, name, tm=256):
    T = qp.shape[0]
    slab = pl.BlockSpec((tm, LANE), lambda i: (i, 0))
    kr_spec = pl.BlockSpec((tm, LANE), lambda i: (i, Z_KR // LANE))
    vec = pl.BlockSpec((1, LANE), lambda i: (0, 0))

    def body(qp_ref, kvp_ref, kr_ref, cos_ref, sin_ref, qg_ref, kg_ref, q_ref, k_ref, v_ref):
        m_n, m_r, first_half = _lane_masks((tm, LANE))
        c = cos_ref[...]
        s = sin_ref[...]
        qg = qg_ref[...]
        kg = kg_ref[...]
        xr = kr_ref[...]
        rr = lax.rsqrt(jnp.sum(xr * xr, axis=-1, keepdims=True) * (1.0 / ROPE) + EPS)
        yr = xr * rr * kg
        kr = jnp.where(m_r, yr * c + _rot(yr, first_half) * s, 0.0)
        for h in range(NH):
            x = qp_ref[:, h * LANE:(h + 1) * LANE]
            x2 = x * x
            rn = lax.rsqrt(jnp.sum(jnp.where(m_n, x2, 0.0), axis=-1, keepdims=True) * (1.0 / NOPE) + EPS)
            rq = lax.rsqrt(jnp.sum(jnp.where(m_r, x2, 0.0), axis=-1, keepdims=True) * (1.0 / ROPE) + EPS)
            y = x * jnp.where(m_n, rn, jnp.where(m_r, rq, 0.0)) * qg
            q_ref[:, h * LANE:(h + 1) * LANE] = (y * c + _rot(y, first_half) * s).astype(q_ref.dtype)
            xk = kvp_ref[:, h * LANE:(h + 1) * LANE]
            rk = lax.rsqrt(jnp.sum(jnp.where(m_n, xk * xk, 0.0), axis=-1, keepdims=True) * (1.0 / NOPE) + EPS)
            k_ref[:, h * LANE:(h + 1) * LANE] = (jnp.where(m_n, xk * rk * kg, 0.0) + kr).astype(k_ref.dtype)
        v_ref[...] = kvp_ref[:, NH * LANE:].astype(v_ref.dtype)

    return _pcall(body, name=name, grid=(T // tm,),
                  in_specs=[_row_spec(tm, NH * LANE), _row_spec(tm, NH * LANE + NH * NOPE), kr_spec, slab, slab, vec, vec],
                  out_specs=[_row_spec(tm, NH * LANE), _row_spec(tm, NH * LANE), _row_spec(tm, NH * NOPE)],
                  out_shape=[_sds((T, NH * LANE), MX), _sds((T, NH * LANE), MX), _sds((T, NH * NOPE), MX)],
                  )(qp, kvp, z, cos, sin, q_gain, k_gain)


def _prep_bwd(dq, dk, dv, qp, kvp, z, cos, sin, q_gain, k_gain, *, nb, name, tm=256):
    T = qp.shape[0]
    tps = (T // nb) // tm
    slab = pl.BlockSpec((tm, LANE), lambda i: (i, 0))
    kr_spec = pl.BlockSpec((tm, LANE), lambda i: (i, Z_KR // LANE))
    vec = pl.BlockSpec((1, LANE), lambda i: (0, 0))

    def body(dq_ref, dk_ref, dv_ref, qp_ref, kvp_ref, kr_ref, cos_ref, sin_ref, qg_ref, kg_ref,
             dqp_ref, dkvp_ref, dkr_ref, st_ref):
        i = pl.program_id(0)
        m_n, m_r, first_half = _lane_masks((tm, LANE))
        c = cos_ref[...]
        s = sin_ref[...]
        qg = qg_ref[...]
        kg = kg_ref[...]
        dqg = jnp.zeros((1, LANE), F32)
        dkg = jnp.zeros((1, LANE), F32)
        dkr_sum = jnp.zeros((tm, LANE), F32)
        for h in range(NH):
            x = qp_ref[:, h * LANE:(h + 1) * LANE]
            x2 = x * x
            rn = lax.rsqrt(jnp.sum(jnp.where(m_n, x2, 0.0), axis=-1, keepdims=True) * (1.0 / NOPE) + EPS)
            rq = lax.rsqrt(jnp.sum(jnp.where(m_r, x2, 0.0), axis=-1, keepdims=True) * (1.0 / ROPE) + EPS)
            rfac = jnp.where(m_n, rn, jnp.where(m_r, rq, 0.0))
            xhat = x * rfac
            do = dq_ref[:, h * LANE:(h + 1) * LANE]
            dy = do * c + _rot_t(do * s, first_half, m_r)
            dqg = dqg + jnp.sum(dy * xhat, axis=0, keepdims=True)
            dxhat = dy * qg
            t = dxhat * xhat
            mean_n = jnp.sum(jnp.where(m_n, t, 0.0), axis=-1, keepdims=True) * (1.0 / NOPE)
            mean_r = jnp.sum(jnp.where(m_r, t, 0.0), axis=-1, keepdims=True) * (1.0 / ROPE)
            dqp_ref[:, h * LANE:(h + 1) * LANE] = (
                rfac * (dxhat - xhat * jnp.where(m_n, mean_n, jnp.where(m_r, mean_r, 0.0)))).astype(dqp_ref.dtype)

            xk = kvp_ref[:, h * LANE:(h + 1) * LANE]
            rk = lax.rsqrt(jnp.sum(jnp.where(m_n, xk * xk, 0.0), axis=-1, keepdims=True) * (1.0 / NOPE) + EPS)
            khat = jnp.where(m_n, xk * rk, 0.0)
            dko = dk_ref[:, h * LANE:(h + 1) * LANE]
            dkn = jnp.where(m_n, dko, 0.0)
            dkg = dkg + jnp.sum(dkn * khat, axis=0, keepdims=True)
            dkhat = dkn * kg
            mean_k = jnp.sum(dkhat * khat, axis=-1, keepdims=True) * (1.0 / NOPE)
            dkvp_ref[:, h * LANE:(h + 1) * LANE] = jnp.where(m_n, rk * (dkhat - khat * mean_k), 0.0).astype(dkvp_ref.dtype)
            dkr_sum = dkr_sum + jnp.where(m_r, dko, 0.0)
        dkvp_ref[:, NH * LANE:] = dv_ref[...].astype(dkvp_ref.dtype)

        xr = kr_ref[...]
        rr = lax.rsqrt(jnp.sum(xr * xr, axis=-1, keepdims=True) * (1.0 / ROPE) + EPS)
        rhat = xr * rr
        dyr = dkr_sum * c + _rot_t(dkr_sum * s, first_half, m_r)
        dkg = dkg + jnp.sum(dyr * rhat, axis=0, keepdims=True)
        drhat = dyr * kg
        mean_kr = jnp.sum(drhat * rhat, axis=-1, keepdims=True) * (1.0 / ROPE)
        dkr_ref[...] = jnp.where(m_r, rr * (drhat - rhat * mean_kr), 0.0).astype(dkr_ref.dtype)

        @pl.when(i % tps == 0)
        def _():
            st_ref[...] = jnp.zeros_like(st_ref)

        st_ref[0, 0:1, :] += dqg
        st_ref[0, 1:2, :] += dkg

    return _pcall(body, name=name, grid=(T // tm,),
                  in_specs=[_row_spec(tm, NH * LANE), _row_spec(tm, NH * LANE), _row_spec(tm, NH * NOPE),
                            _row_spec(tm, NH * LANE), _row_spec(tm, NH * LANE + NH * NOPE), kr_spec, slab, slab, vec, vec],
                  out_specs=[_row_spec(tm, NH * LANE), _row_spec(tm, NH * LANE + NH * NOPE), slab,
                             pl.BlockSpec((1, SUBLANE, LANE), lambda i: (i // tps, 0, 0))],
                  out_shape=[_sds((T, NH * LANE), MX), _sds((T, NH * LANE + NH * NOPE), MX), _sds((T, LANE), MX),
                             _sds((nb, SUBLANE, LANE), F32)],
                  )(dq, dk, dv, qp, kvp, z, cos, sin, q_gain, k_gain)


def _lower_triangle(t):
    return lax.broadcasted_iota(jnp.int32, (t, t), 1) <= lax.broadcasted_iota(jnp.int32, (t, t), 0)


def _attn_fwd(q, k, v, *, nb, name, tq=512):
    T = q.shape[0]
    S = T // nb
    tq = _pick(S, tq)
    nq = S // tq
    tk = tq
    npair = NH // 2

    def body(q_ref, k_ref, v_ref, o_ref, lse_ref):
        qi = pl.program_id(2)
        lane = lax.broadcasted_iota(jnp.int32, (tq, LANE), 1)
        qs = [q_ref[:, hh * LANE:(hh + 1) * LANE] for hh in range(2)]

        def block(j, carry, diagonal):
            k0 = pl.multiple_of(j * tk, tk)
            vb = v_ref[pl.ds(k0, tk), :]
            new = []
            for hh in range(2):
                m, l, acc = carry[hh]
                kb = k_ref[pl.ds(k0, tk), hh * LANE:(hh + 1) * LANE]
                s = lax.dot_general(qs[hh], kb, (((1,), (1,)), ((), ())), preferred_element_type=F32) * ATTN_SCALE
                if diagonal:
                    s = jnp.where(_lower_triangle(tq), s, NEG)
                m_new = jnp.maximum(m, jnp.max(s, axis=-1, keepdims=True))
                p = jnp.exp(s - m_new)
                alpha = jnp.exp(m - m_new)
                l = alpha * l + jnp.sum(p, axis=-1, keepdims=True)
                acc = alpha * acc + jnp.dot(p.astype(MX), vb, preferred_element_type=F32)
                new.append((m_new, l, acc))
            return tuple(new)

        init = tuple((jnp.full((tq, 1), NEG, F32), jnp.zeros((tq, 1), F32), jnp.zeros((tq, LANE), F32)) for _ in range(2))
        carry = lax.fori_loop(0, qi, lambda j, c: block(j, c, False), init)
        (m0, l0, acc0), (m1, l1, acc1) = block(qi, carry, True)
        o_ref[...] = jnp.where(lane < NOPE, acc0 / l0, acc1 / l1).astype(o_ref.dtype)
        lse_ref[...] = jnp.where(lane < NOPE, m0 + jnp.log(l0), m1 + jnp.log(l1))

    return _pcall(body, name=name, grid=(nb, npair, nq),
                  in_specs=[pl.BlockSpec((tq, 2 * LANE), lambda b, p, i: (b * nq + i, p)),
                            pl.BlockSpec((S, 2 * LANE), lambda b, p, i: (b, p)),
                            pl.BlockSpec((S, LANE), lambda b, p, i: (b, p))],
                  out_specs=[pl.BlockSpec((tq, LANE), lambda b, p, i: (b * nq + i, p)),
                             pl.BlockSpec((tq, LANE), lambda b, p, i: (b * nq + i, p))],
                  out_shape=[_sds((T, NH * NOPE), MX), _sds((T, NH * NOPE), F32)])(q, k, v)


def _attn_bwd(q, k, v, o, lse, do, *, nb, name, tq=512):
    T = q.shape[0]
    S = T // nb
    tq = _pick(S, tq)
    nq = S // tq
    tk = tq
    npair = NH // 2

    def body(q_ref, k_ref, v_ref, o_ref, lse_ref, do_ref, dq_ref, dk_ref, dv_ref, delta_ref):
        lane = lax.broadcasted_iota(jnp.int32, (tq, LANE), 1)
        first = lane < NOPE
        dq_ref[...] = jnp.zeros_like(dq_ref)

        def delta_step(qi, _):
            q0 = pl.multiple_of(qi * tq, tq)
            prod = do_ref[pl.ds(q0, tq), :] * o_ref[pl.ds(q0, tq), :].astype(F32)
            d0 = jnp.sum(jnp.where(first, prod, 0.0), axis=-1, keepdims=True)
            d1 = jnp.sum(jnp.where(first, 0.0, prod), axis=-1, keepdims=True)
            delta_ref[pl.ds(q0, tq), :] = jnp.where(first, d0, d1)
            return 0

        lax.fori_loop(0, nq, delta_step, 0)

        def kv_step(kj, _):
            k0 = pl.multiple_of(kj * tk, tk)
            kbs = [k_ref[pl.ds(k0, tk), hh * LANE:(hh + 1) * LANE] for hh in range(2)]
            vb = v_ref[pl.ds(k0, tk), :]

            def q_block(qi, carry, diagonal):
                dk0, dk1, dv = carry
                dks = [dk0, dk1]
                q0 = pl.multiple_of(qi * tq, tq)
                dov = do_ref[pl.ds(q0, tq), :]
                lse_v = lse_ref[pl.ds(q0, tq), :]
                delta_v = delta_ref[pl.ds(q0, tq), :]
                for hh in range(2):
                    qh = q_ref[pl.ds(q0, tq), hh * LANE:(hh + 1) * LANE]
                    dob = jnp.where(first if hh == 0 else jnp.logical_not(first), dov, 0.0).astype(MX)
                    s = lax.dot_general(qh, kbs[hh], (((1,), (1,)), ((), ())), preferred_element_type=F32) * ATTN_SCALE
                    p = jnp.exp(s - lse_v[:, hh * NOPE:hh * NOPE + 1])
                    if diagonal:
                        p = jnp.where(_lower_triangle(tq), p, 0.0)
                    dp = lax.dot_general(dob, vb, (((1,), (1,)), ((), ())), preferred_element_type=F32)
                    ds = (p * (dp - delta_v[:, hh * NOPE:hh * NOPE + 1]) * ATTN_SCALE).astype(MX)
                    dks[hh] = dks[hh] + lax.dot_general(ds, qh, (((0,), (0,)), ((), ())), preferred_element_type=F32)
                    dv = dv + lax.dot_general(p.astype(MX), dob, (((0,), (0,)), ((), ())), preferred_element_type=F32)
                    dq_ref[pl.ds(q0, tq), hh * LANE:(hh + 1) * LANE] += jnp.dot(ds, kbs[hh], preferred_element_type=F32)
                return dks[0], dks[1], dv

            zero = jnp.zeros((tk, LANE), F32)
            carry = q_block(kj, (zero, zero, zero), True)
            dk0, dk1, dv = lax.fori_loop(kj + 1, nq, lambda qi, c: q_block(qi, c, False), carry)
            dk_ref[pl.ds(k0, tk), 0:LANE] = dk0
            dk_ref[pl.ds(k0, tk), LANE:2 * LANE] = dk1
            dv_ref[pl.ds(k0, tk), :] = dv
            return 0

        lax.fori_loop(0, nq, kv_step, 0)

    pair256 = pl.BlockSpec((S, 2 * LANE), lambda b, p: (b, p))
    pair128 = pl.BlockSpec((S, LANE), lambda b, p: (b, p))
    return _pcall(body, name=name, grid=(nb, npair),
                  in_specs=[pair256, pair256, pair128, pair128, pair128, pair128],
                  out_specs=[pair256, pair256, pair128],
                  out_shape=[_sds((T, NH * LANE), F32), _sds((T, NH * LANE), F32), _sds((T, NH * NOPE), F32)],
                  scratch=[pltpu.VMEM((S, LANE), F32)])(q, k, v, o, lse, do)


def _layer_fwd_bwd(x, tgt, mod3, cos, sin, W, P):
    nb = mod3.shape[0]
    h1 = _fwd_block(x, None, mod3, P["norm_ffn1"], sub=0, coef=0.0, name="fwd_norm1")
    gu1, a1 = _ffn_up(h1, W["ffn1_in"], name="ffn1_up")
    f1 = _mm(a1, W["ffn1_out"], mode="nn", out_dtype=F32, name="ffn1_out", tk=DFF)
    x1, h2 = _fwd_block(x, f1, mod3, P["norm_mix"], sub=1, coef=0.5, name="fwd_norm2")
    z = _mm(h2, W["w_in"], mode="nn", out_dtype=F32, name="mix_in", tn=1664)
    pooled, mixed, scaled = _pool_fwd(z, W["pool_grp"], P["pool_scale"], nb=nb, name="pool_fwd")
    br_pool = _mm(scaled, W["pool_proj"], mode="nn", out_dtype=F32, name="pool_proj")
    qn, kvn = _lat_fwd(z, P["q_a_norm"], P["kv_a_norm"], name="lat_fwd")
    qp = _mm(qn, W["q_up"], mode="nn", out_dtype=F32, name="q_up")
    kvp = _mm(kvn, W["kv_up"], mode="nn", out_dtype=F32, name="kv_up")
    q, k, v = _prep_fwd(qp, kvp, z, cos, sin, P["q_gain"], P["k_gain"], name="prep_fwd")
    attn, lse = _attn_fwd(q, k, v, nb=nb, name="attn_fwd")
    br_mla = _mm(attn, W["mla_proj"], mode="nn", out_dtype=F32, name="mla_proj")
    merged = _merge_fwd(z, br_pool, br_mla, name="merge_fwd")
    mo = _mm(merged, W["w_out"], mode="nn", out_dtype=F32, name="mix_out")
    x2, h3 = _fwd_block(x1, mo, mod3, P["norm_ffn2"], sub=2, coef=1.0, name="fwd_norm3")
    gu2, a2 = _ffn_up(h3, W["ffn2_in"], name="ffn2_up")
    f2 = _mm(a2, W["ffn2_out"], mode="nn", out_dtype=F32, name="ffn2_out", tk=DFF)
    dy, df2, st_fin, loss = _final(x2, f2, tgt, mod3, name="loss_head")

    G = {}
    tokens = x.shape[0]
    G["ffn2_out"] = _mm(a2, df2, mode="tn", out_dtype=F32, name="d_ffn2_out", tm=256, tn=D, tk=tokens, n_outer=True)
    dgu2 = _ffn_dact(df2, gu2, W["ffn2_out"], name="ffn2_dact")
    dh3 = _ffn_dh(dgu2, W["ffn2_in"], name="d_ffn2_h")
    G["ffn2_in"] = _ffn_dw_in(h3, dgu2, name="d_ffn2_in")
    dx2, dmo, st3 = _bwd_block(x2, dh3, dy, mo, mod3, P["norm_ffn2"], sub=2, coef=1.0, name="bwd_norm3")
    G["w_out"] = _mm(merged, dmo, mode="tn", out_dtype=F32, name="d_mix_out")
    dmerged = _mm(dmo, W["w_out"], mode="nt", out_dtype=F32, name="d_merged")
    dbr_pool, dbr_mla, dgates = _merge_bwd(z, br_pool, br_mla, dmerged, name="merge_bwd")
    G["pool_proj"] = _mm(scaled, dbr_pool, mode="tn", out_dtype=F32, name="d_pool_proj")
    dscaled = _mm(dbr_pool, W["pool_proj"], mode="nt", out_dtype=F32, name="d_pool_scaled")
    du_pool, d_pool_grp, d_pool_scale = _pool_bwd(dscaled, mixed, pooled, W["pool_grp"], P["pool_scale"], nb=nb, name="pool_bwd")
    G["mla_proj"] = _mm(attn, dbr_mla, mode="tn", out_dtype=F32, name="d_mla_proj")
    dattn = _mm(dbr_mla, W["mla_proj"], mode="nt", out_dtype=F32, name="d_attn")
    dq, dk, dv = _attn_bwd(q, k, v, attn, lse, dattn, nb=nb, name="attn_bwd")
    dqp, dkvp, dkr, st_prep = _prep_bwd(dq, dk, dv, qp, kvp, z, cos, sin, P["q_gain"], P["k_gain"], nb=nb, name="prep_bwd")
    G["q_up"] = _mm(qn, dqp, mode="tn", out_dtype=F32, name="d_q_up")
    dqn = _mm(dqp, W["q_up"], mode="nt", out_dtype=F32, name="d_qn")
    G["kv_up"] = _mm(kvn, dkvp, mode="tn", out_dtype=F32, name="d_kv_up")
    dkvn = _mm(dkvp, W["kv_up"], mode="nt", out_dtype=F32, name="d_kvn", tk=1536)
    dql, dkvl, st_q, st_kv = _lat_bwd(z, dqn, dkvn, P["q_a_norm"], P["kv_a_norm"], nb=nb, name="lat_bwd")
    dz = jnp.concatenate([du_pool, dql, dkvl, dkr, dgates], axis=1)
    G["w_in"] = _mm(h2, dz, mode="tn", out_dtype=F32, name="d_mix_in", tn=1664)
    dh2 = _mm(dz, W["w_in"], mode="nt", out_dtype=F32, name="d_mix_h", tk=1664)
    dx1, df1, st2 = _bwd_block(x1, dh2, dx2, f1, mod3, P["norm_mix"], sub=1, coef=0.5, name="bwd_norm2")
    G["ffn1_out"] = _mm(a1, df1, mode="tn", out_dtype=F32, name="d_ffn1_out", tm=256, tn=D, tk=tokens, n_outer=True)
    dgu1 = _ffn_dact(df1, gu1, W["ffn1_out"], name="ffn1_dact")
    dh1 = _ffn_dh(dgu1, W["ffn1_in"], name="d_ffn1_h")
    G["ffn1_in"] = _ffn_dw_in(h1, dgu1, name="d_ffn1_in")
    grad_x, st1 = _bwd_block(x, dh1, dx1, None, mod3, P["norm_ffn1"], sub=0, coef=0.0, name="bwd_norm1")

    dmod = jnp.stack([st1[:, 0], st1[:, 1], st2[:, 3], st2[:, 0], st2[:, 1], st3[:, 3], st3[:, 0], st3[:, 1], st_fin[:, 0]],
                     axis=1)
    small = {
        "norm_ffn1": st1[:, 2], "norm_mix": st2[:, 2], "norm_ffn2": st3[:, 2],
        "pool_grp": d_pool_grp, "pool_scale": d_pool_scale[:, 0],
        "q_a_norm": st_q[:, 0], "kv_a_norm": st_kv[:, 0],
        "q_gain": st_prep[:, 0], "k_gain": st_prep[:, 1],
    }
    return loss, grad_x, G, small, dmod


def _w_in_to_kernel(w):
    k = w.shape[0]
    zeros = lambda n: jnp.zeros((k, n), w.dtype)
    return jnp.concatenate([w[:, 0:1152], zeros(KR_LANE), w[:, 1152:1184], zeros(LANE - KR_LANE - ROPE), w[:, 1184:]], axis=1)


def _w_in_from_kernel(g):
    return jnp.concatenate([g[:, 0:1152], g[:, Z_KR + KR_LANE:Z_KR + KR_LANE + ROPE], g[:, Z_GP:]], axis=1)


def _q_up_to_kernel(w):
    k = w.shape[0]
    return jnp.pad(w.reshape(k, NH, NOPE + ROPE), ((0, 0), (0, 0), (0, LANE - NOPE - ROPE))).reshape(k, NH * LANE)


def _q_up_from_kernel(g):
    k = g.shape[0]
    return g.reshape(k, NH, LANE)[:, :, :NOPE + ROPE].reshape(k, NH * (NOPE + ROPE))


def _kv_up_to_kernel(w):
    k = w.shape[0]
    w3 = w.reshape(k, NH, 2 * NOPE)
    kpart = jnp.pad(w3[:, :, :NOPE], ((0, 0), (0, 0), (0, LANE - NOPE))).reshape(k, NH * LANE)
    return jnp.concatenate([kpart, w3[:, :, NOPE:].reshape(k, NH * NOPE)], axis=1)


def _kv_up_from_kernel(g):
    k = g.shape[0]
    kpart = g[:, :NH * LANE].reshape(k, NH, LANE)[:, :, :NOPE]
    vpart = g[:, NH * LANE:].reshape(k, NH, NOPE)
    return jnp.concatenate([kpart, vpart], axis=2).reshape(k, NH * 2 * NOPE)


def _gain_slab(nope, rope):
    return jnp.concatenate([nope, rope, jnp.zeros((1, LANE - NOPE - ROPE), nope.dtype)], axis=1)


def _rope_tables(positions):
    inv_freq = 10000.0 ** (-jnp.arange(0, ROPE, 2, dtype=F32) / ROPE)
    ang = positions.astype(F32)[:, None] * inv_freq
    ang = jnp.concatenate([ang, ang], axis=-1)
    t = positions.shape[0]
    cos = jnp.concatenate([jnp.ones((t, KR_LANE), F32), jnp.cos(ang), jnp.ones((t, LANE - KR_LANE - ROPE), F32)], axis=1)
    sin = jnp.concatenate([jnp.zeros((t, KR_LANE), F32), jnp.sin(ang), jnp.zeros((t, LANE - KR_LANE - ROPE), F32)], axis=1)
    return cos, sin


def _coords():
    return lax.axis_index("x"), lax.axis_index("y"), lax.axis_index("c")


HBM_SPEC = pl.BlockSpec(memory_space=pl.ANY)
VMEM_SPEC = pl.BlockSpec(memory_space=pltpu.VMEM)


def _gather8(v, *, name):
    r, c = v.shape

    def body(v_ref, out_ref, send_sems, recv_sems, local_sem):
        x, y, cc = _coords()
        me = 4 * x + 2 * y + cc
        mine = pltpu.make_async_copy(v_ref, out_ref.at[me], local_sem)
        mine.start()
        copies = []
        for kk in range(1, N_DEV):
            peer = (x ^ (kk >> 2), y ^ ((kk >> 1) & 1), cc ^ (kk & 1))
            cp = pltpu.make_async_remote_copy(src_ref=v_ref, dst_ref=out_ref.at[me], send_sem=send_sems.at[kk - 1],
                                              recv_sem=recv_sems.at[kk - 1], device_id=peer, device_id_type=MESH)
            cp.start()
            copies.append(cp)
        for kk in range(1, N_DEV):
            peer_slot = me ^ kk
            pltpu.make_async_remote_copy(src_ref=v_ref, dst_ref=out_ref.at[peer_slot], send_sem=send_sems.at[kk - 1],
                                         recv_sem=recv_sems.at[kk - 1], device_id=(x, y, cc), device_id_type=MESH).wait_recv()
        for cp in copies:
            cp.wait_send()
        mine.wait()

    return _pcall(body, name=name, out_shape=_sds((N_DEV, r, c), v.dtype), in_specs=[VMEM_SPEC], out_specs=VMEM_SPEC,
                  scratch=[pltpu.SemaphoreType.DMA((N_DEV - 1,)), pltpu.SemaphoreType.DMA((N_DEV - 1,)), pltpu.SemaphoreType.DMA])(v)


CHIP_RELS = ((1, 0), (0, 1), (1, 1))


def _gather_weights(shards, *, name):
    n = len(shards)

    def body(*refs):
        ins, outs = refs[:n], refs[n:2 * n]
        isend, irecv, dsend, drecv, lsem = refs[2 * n:]
        x, y, cc = _coords()
        chip = 2 * x + y
        sib = (x, y, 1 - cc)
        local_copies, sends = [], []
        for a in range(n):
            half = ins[a].shape[0] // 2
            mine = pl.ds(cc * half, half)
            loc = pltpu.make_async_copy(ins[a], outs[a].at[chip], lsem.at[a])
            loc.start()
            local_copies.append(loc)
            for j, (dx, dy) in enumerate(CHIP_RELS):
                cp = pltpu.make_async_remote_copy(src_ref=ins[a].at[mine], dst_ref=outs[a].at[chip, mine],
                                                  send_sem=isend.at[a, j], recv_sem=irecv.at[a, j],
                                                  device_id=(x ^ dx, y ^ dy, cc), device_id_type=MESH)
                cp.start()
                sends.append(cp)
        for a in range(n):
            half = ins[a].shape[0] // 2
            mine = pl.ds(cc * half, half)
            for j, (dx, dy) in enumerate(CHIP_RELS):
                src_chip = 2 * (x ^ dx) + (y ^ dy)
                landed = outs[a].at[src_chip, mine]
                pltpu.make_async_remote_copy(src_ref=landed, dst_ref=landed, send_sem=isend.at[a, j], recv_sem=irecv.at[a, j],
                                             device_id=sib, device_id_type=MESH).wait_recv()
                fwd = pltpu.make_async_remote_copy(src_ref=landed, dst_ref=landed, send_sem=dsend.at[a, j],
                                                   recv_sem=drecv.at[a, j], device_id=sib, device_id_type=MESH)
                fwd.start()
                sends.append(fwd)
        for a in range(n):
            half = ins[a].shape[0] // 2
            theirs = pl.ds((1 - cc) * half, half)
            for j, (dx, dy) in enumerate(CHIP_RELS):
                src_chip = 2 * (x ^ dx) + (y ^ dy)
                landed = outs[a].at[src_chip, theirs]
                pltpu.make_async_remote_copy(src_ref=landed, dst_ref=landed, send_sem=dsend.at[a, j], recv_sem=drecv.at[a, j],
                                             device_id=sib, device_id_type=MESH).wait_recv()
        for cp in sends:
            cp.wait_send()
        for loc in local_copies:
            loc.wait()

    return _pcall(body, name=name, out_shape=[_sds((N_CHIP,) + s.shape, s.dtype) for s in shards],
                  in_specs=[HBM_SPEC] * n, out_specs=[HBM_SPEC] * n,
                  scratch=[pltpu.SemaphoreType.DMA((n, 3)), pltpu.SemaphoreType.DMA((n, 3)), pltpu.SemaphoreType.DMA((n, 3)),
                           pltpu.SemaphoreType.DMA((n, 3)), pltpu.SemaphoreType.DMA((n,))])(*shards)


def _swap_halves(parts, *, name):
    n = len(parts)

    def body(*refs):
        ins, outs = refs[:n], refs[n:2 * n]
        ssem, rsem = refs[2 * n:]
        x, y, cc = _coords()
        sib = (x, y, 1 - cc)
        cps = []
        for a in range(n):
            half = ins[a].shape[1] // 2
            cp = pltpu.make_async_remote_copy(src_ref=ins[a].at[:, pl.ds((1 - cc) * half, half)], dst_ref=outs[a],
                                              send_sem=ssem.at[a], recv_sem=rsem.at[a], device_id=sib, device_id_type=MESH)
            cp.start()
            cps.append(cp)
        for cp in cps:
            cp.wait()

    return _pcall(body, name=name,
                  out_shape=[_sds((p.shape[0], p.shape[1] // 2, p.shape[2]), p.dtype) for p in parts],
                  in_specs=[HBM_SPEC] * n, out_specs=[HBM_SPEC] * n,
                  scratch=[pltpu.SemaphoreType.DMA((n,)), pltpu.SemaphoreType.DMA((n,))])(*parts)


def _add_half(full, other, cidx, *, name):
    nch, r, c = full.shape
    half = r // 2
    tr = _pick_rows(half)
    nbk = half // tr
    grid_spec = pltpu.PrefetchScalarGridSpec(
        num_scalar_prefetch=1, grid=(nch, nbk),
        in_specs=[pl.BlockSpec((1, tr, c), lambda j, i, cref: (j, cref[0] * nbk + i, 0)),
                  pl.BlockSpec((1, tr, c), lambda j, i, cref: (j, i, 0))],
        out_specs=pl.BlockSpec((1, tr, c), lambda j, i, cref: (j, i, 0)))

    def body(cref, a_ref, b_ref, o_ref):
        o_ref[...] = (a_ref[...] + b_ref[...]).astype(o_ref.dtype)

    return _pcall(body, name=name, out_shape=_sds((nch, half, c), MX), grid_spec=grid_spec)(cidx, full, other)


def _pick_rows(rows, target=512):
    best = None
    for t in range(16, min(rows, target) + 1, 16):
        if rows % t == 0:
            best = t
    return rows if best is None else best


def _exchange_chips(parts, *, name):
    n = len(parts)

    def body(*refs):
        ins, outs = refs[:n], refs[n:2 * n]
        ssem, rsem, lsem = refs[2 * n:]
        x, y, cc = _coords()
        chip = 2 * x + y
        cps = []
        for a in range(n):
            loc = pltpu.make_async_copy(ins[a].at[chip], outs[a].at[chip], lsem.at[a])
            loc.start()
            cps.append((loc, None))
            for j, (dx, dy) in enumerate(CHIP_RELS):
                peer_chip = 2 * (x ^ dx) + (y ^ dy)
                cp = pltpu.make_async_remote_copy(src_ref=ins[a].at[peer_chip], dst_ref=outs[a].at[chip],
                                                  send_sem=ssem.at[a, j], recv_sem=rsem.at[a, j],
                                                  device_id=(x ^ dx, y ^ dy, cc), device_id_type=MESH)
                cp.start()
                cps.append((cp, peer_chip))
        for a in range(n):
            for j, (dx, dy) in enumerate(CHIP_RELS):
                peer_chip = 2 * (x ^ dx) + (y ^ dy)
                pltpu.make_async_remote_copy(src_ref=ins[a].at[peer_chip], dst_ref=outs[a].at[peer_chip], send_sem=ssem.at[a, j],
                                             recv_sem=rsem.at[a, j], device_id=(x, y, cc), device_id_type=MESH).wait_recv()
        for cp, peer_chip in cps:
            if peer_chip is None:
                cp.wait()
            else:
                cp.wait_send()

    return _pcall(body, name=name, out_shape=[_sds(p.shape, p.dtype) for p in parts],
                  in_specs=[HBM_SPEC] * n, out_specs=[HBM_SPEC] * n,
                  scratch=[pltpu.SemaphoreType.DMA((n, 3)), pltpu.SemaphoreType.DMA((n, 3)), pltpu.SemaphoreType.DMA((n,))])(*parts)


def _sum_chips(q, cidx, *, name):
    nch, h, c = q.shape
    tr = _pick_rows(h)
    nbk = h // tr
    grid_spec = pltpu.PrefetchScalarGridSpec(
        num_scalar_prefetch=1, grid=(nbk,),
        in_specs=[pl.BlockSpec((nch, tr, c), lambda i, cref: (0, i, 0))],
        out_specs=pl.BlockSpec((tr, c), lambda i, cref: (cref[0] * nbk + i, 0)))

    def body(cref, q_ref, o_ref):
        acc = q_ref[0].astype(F32) + q_ref[1].astype(F32)
        acc = acc + q_ref[2].astype(F32)
        o_ref[...] = acc + q_ref[3].astype(F32)

    return _pcall(body, name=name, out_shape=_sds((2 * h, c), F32), grid_spec=grid_spec)(cidx, q)


def _join_halves(fulls, *, name):
    n = len(fulls)

    def body(*refs):
        outs = refs[n:2 * n]
        ssem, rsem = refs[2 * n:]
        x, y, cc = _coords()
        sib = (x, y, 1 - cc)
        cps = []
        for a in range(n):
            h = outs[a].shape[0] // 2
            mine = outs[a].at[pl.ds(cc * h, h)]
            cp = pltpu.make_async_remote_copy(src_ref=mine, dst_ref=mine, send_sem=ssem.at[a], recv_sem=rsem.at[a],
                                              device_id=sib, device_id_type=MESH)
            cp.start()
            cps.append(cp)
        for a in range(n):
            h = outs[a].shape[0] // 2
            theirs = outs[a].at[pl.ds((1 - cc) * h, h)]
            pltpu.make_async_remote_copy(src_ref=theirs, dst_ref=theirs, send_sem=ssem.at[a], recv_sem=rsem.at[a],
                                         device_id=sib, device_id_type=MESH).wait_recv()
        for cp in cps:
            cp.wait_send()

    return _pcall(body, name=name, out_shape=[_sds(p.shape, p.dtype) for p in fulls],
                  in_specs=[HBM_SPEC] * n, out_specs=[HBM_SPEC] * n, aliases={a: a for a in range(n)},
                  scratch=[pltpu.SemaphoreType.DMA((n,)), pltpu.SemaphoreType.DMA((n,))])(*fulls)


def _ada_fwd(c_all, w, b, *, name):
    m, kdim = c_all.shape
    n = w.shape[1]
    tn = _pick(n, 1152)

    def body(c_ref, w_ref, b_ref, o_ref):
        cv = c_ref[...]
        act = (cv * _sigmoid(cv)).astype(MX)
        o_ref[...] = jnp.dot(act, w_ref[...].astype(MX), preferred_element_type=F32) + b_ref[...]

    return _pcall(body, name=name, out_shape=_sds((m, n), F32), grid=(n // tn,),
                  in_specs=[pl.BlockSpec((m, kdim), lambda j: (0, 0)), pl.BlockSpec((kdim, tn), lambda j: (0, j)),
                            pl.BlockSpec((1, tn), lambda j: (0, j))],
                  out_specs=pl.BlockSpec((m, tn), lambda j: (0, j)))(c_all, w, b)


def _ada_bwd(c_all, dmod_cols, *, name):
    m, kdim = c_all.shape
    n = dmod_cols.shape[1]
    tn = _pick(n, 1152)

    def body(c_ref, d_ref, o_ref):
        cv = c_ref[...]
        act = (cv * _sigmoid(cv)).astype(MX)
        o_ref[...] = lax.dot_general(act, d_ref[...].astype(MX), (((0,), (0,)), ((), ())), preferred_element_type=F32)

    return _pcall(body, name=name, out_shape=_sds((kdim, n), F32), grid=(n // tn,),
                  in_specs=[pl.BlockSpec((m, kdim), lambda j: (0, 0)), pl.BlockSpec((m, tn), lambda j: (0, j))],
                  out_specs=pl.BlockSpec((kdim, tn), lambda j: (0, j)))(c_all, dmod_cols)


def _sum_small(v, nseq, rows, *, name):
    n, r, c = v.shape
    extra = r - nseq * rows

    def body(v_ref, o_ref):
        acc = None
        for d in range(n):
            for s in range(nseq):
                t = v_ref[d, s * rows:(s + 1) * rows, :]
                acc = t if acc is None else acc + t
        o_ref[0:rows, :] = acc
        acc = v_ref[0, nseq * rows:, :]
        for d in range(1, n):
            acc = acc + v_ref[d, nseq * rows:, :]
        o_ref[rows:, :] = acc

    return _pcall(body, name=name, out_shape=_sds((rows + extra, c), F32), in_specs=[VMEM_SPEC], out_specs=VMEM_SPEC)(v)


def _adamw(w, g, m, v, *, name):
    r, c = w.shape
    tr = _pick_rows(r, 256)
    spec = pl.BlockSpec((tr, c), lambda i: (i, 0))
    bc1 = 1.0 / (1.0 - ADAM_B1 ** ADAM_STEP)
    bc2 = 1.0 / (1.0 - ADAM_B2 ** ADAM_STEP)

    def body(w_ref, g_ref, m_ref, v_ref, d_ref, mo_ref, vo_ref):
        gv = g_ref[...]
        mn = ADAM_B1 * m_ref[...] + (1.0 - ADAM_B1) * gv
        vn = ADAM_B2 * v_ref[...] + (1.0 - ADAM_B2) * (gv * gv)
        mo_ref[...] = mn
        vo_ref[...] = vn
        d_ref[...] = -ADAM_LR * ((mn * bc1) / (jnp.sqrt(vn * bc2) + ADAM_EPS) + ADAM_WD * w_ref[...])

    out = _sds((r, c), F32)
    return _pcall(body, name=name, out_shape=[out, out, out], grid=(r // tr,), in_specs=[spec] * 4, out_specs=[spec] * 3)(w, g, m, v)


BIG = ("w_ffn1_in", "w_ffn1_out", "w_in", "w_pool_proj", "w_q_up", "w_kv_up", "w_mla_proj", "w_out", "w_ffn2_in", "w_ffn2_out")
ROW_SHARDED = ("w_ffn1_out", "w_out", "w_ffn2_out")
KERNEL_NAME = {"w_ffn1_in": "ffn1_in", "w_ffn1_out": "ffn1_out", "w_in": "w_in", "w_pool_proj": "pool_proj", "w_q_up": "q_up",
               "w_kv_up": "kv_up", "w_mla_proj": "mla_proj", "w_out": "w_out", "w_ffn2_in": "ffn2_in", "w_ffn2_out": "ffn2_out"}
WEIGHTS = ("w_ada", "b_ada", "norm_ffn1", "w_ffn1_in", "w_ffn1_out", "norm_mix", "w_in", "pool_grp", "pool_scale", "w_pool_proj",
           "q_a_norm", "w_q_up", "kv_a_norm", "w_kv_up", "q_norm_nope", "q_norm_rope", "k_norm_nope", "k_norm_rope", "w_mla_proj",
           "w_out", "norm_ffn2", "w_ffn2_in", "w_ffn2_out")
SMALL = ("b_ada", "norm_ffn1", "norm_mix", "pool_grp", "pool_scale", "q_a_norm", "kv_a_norm", "q_norm_nope", "q_norm_rope",
         "k_norm_nope", "k_norm_rope", "norm_ffn2")
SMALL_SEQ = tuple(n for n in SMALL if n != "pool_grp")
SLAB_W = 1024


def _assemble(name, stacked):
    if name in ROW_SHARDED:
        return stacked.reshape(stacked.shape[0] * stacked.shape[1], stacked.shape[2])
    return jnp.transpose(stacked, (1, 0, 2)).reshape(stacked.shape[1], stacked.shape[0] * stacked.shape[2])


def _split(name, full):
    if name in ROW_SHARDED:
        return full.reshape(N_CHIP, full.shape[0] // N_CHIP, full.shape[1])
    return jnp.transpose(full.reshape(full.shape[0], N_CHIP, full.shape[1] // N_CHIP), (1, 0, 2))


GU = ("w_ffn1_in", "w_ffn2_in")


def _to_rows(v, width=SLAB_W):
    flat = v.reshape(-1)
    rows = -(-flat.shape[0] // width)
    return jnp.pad(flat, (0, rows * width - flat.shape[0])).reshape(rows, width)


def _pack_small(parts, names=SMALL):
    rows, spans, at = [], {}, 0
    for name in names:
        r = _to_rows(parts[name])
        spans[name] = (at, parts[name].size, parts[name].shape)
        rows.append(r)
        at += r.shape[0]
    pad = (-at) % SUBLANE
    if pad:
        rows.append(jnp.zeros((pad, SLAB_W), F32))
    return jnp.concatenate(rows, axis=0), spans


def _unpack_small(slab, spans):
    out = {}
    for name, (at, size, shape) in spans.items():
        nrow = -(-size // SLAB_W)
        out[name] = slab[at:at + nrow].reshape(-1)[:size].reshape(shape)
    return out


def kernel(x, c, positions, w_ada, b_ada, norm_ffn1, w_ffn1_in, w_ffn1_out, norm_mix, w_in, pool_grp, pool_scale, w_pool_proj, q_a_norm, w_q_up, kv_a_norm, w_kv_up, q_norm_nope, q_norm_rope, k_norm_nope, k_norm_rope, w_mla_proj, w_out, norm_ffn2, w_ffn2_in, w_ffn2_out, loss_target, m_w_ada, m_b_ada, m_norm_ffn1, m_w_ffn1_in, m_w_ffn1_out, m_norm_mix, m_w_in, m_pool_grp, m_pool_scale, m_w_pool_proj, m_q_a_norm, m_w_q_up, m_kv_a_norm, m_w_kv_up, m_q_norm_nope, m_q_norm_rope, m_k_norm_nope, m_k_norm_rope, m_w_mla_proj, m_w_out, m_norm_ffn2, m_w_ffn2_in, m_w_ffn2_out, v_w_ada, v_b_ada, v_norm_ffn1, v_w_ffn1_in, v_w_ffn1_out, v_norm_mix, v_w_in, v_pool_grp, v_pool_scale, v_w_pool_proj, v_q_a_norm, v_w_q_up, v_kv_a_norm, v_w_kv_up, v_q_norm_nope, v_q_norm_rope, v_k_norm_nope, v_k_norm_rope, v_w_mla_proj, v_w_out, v_norm_ffn2, v_w_ffn2_in, v_w_ffn2_out):
    args = dict(locals())
    wts = {n: args[n][0] for n in WEIGHTS}
    mom = {n: args["m_" + n][0] for n in WEIGHTS}
    var = {n: args["v_" + n][0] for n in WEIGHTS}
    nb, seq, dm = x.shape
    tokens = nb * seq
    xi, yi, ci = _coords()
    chip = 2 * xi + yi
    dev = 4 * xi + 2 * yi + ci

    c_all = _gather8(c, name="gather_c").reshape(N_DEV * nb, dm)
    ncol = w_ada.shape[2]
    b_cols = lax.dynamic_slice_in_dim(wts["b_ada"].reshape(1, -1), chip * ncol, ncol, axis=1)
    mod_cols = _ada_fwd(c_all, wts["w_ada"], b_cols, name="ada_fwd")
    mod_all = _gather8(mod_cols, name="gather_mod")
    mine = lax.dynamic_slice_in_dim(mod_all, dev * nb, nb, axis=1)
    mod = jnp.concatenate([mine[0], mine[2], mine[4], mine[6]], axis=1)
    mod3 = mod.reshape(nb, 9, dm)

    gathered = _gather_weights([wts[n].astype(MX) for n in BIG], name="gather_weights")
    full = {n: _assemble(n, g) for n, g in zip(BIG, gathered) if n not in GU}
    W = {KERNEL_NAME[n]: full[n] for n in BIG if n not in GU}
    for n, g in zip(BIG, gathered):
        if n in GU:
            W[KERNEL_NAME[n]] = g
    W["w_in"] = _w_in_to_kernel(full["w_in"])
    W["q_up"] = _q_up_to_kernel(full["w_q_up"])
    W["kv_up"] = _kv_up_to_kernel(full["w_kv_up"])
    W["pool_grp"] = wts["pool_grp"].astype(MX)
    P = {"norm_ffn1": wts["norm_ffn1"].reshape(1, dm), "norm_mix": wts["norm_mix"].reshape(1, dm),
         "norm_ffn2": wts["norm_ffn2"].reshape(1, dm), "pool_scale": wts["pool_scale"].reshape(1, POOL_W),
         "q_a_norm": wts["q_a_norm"].reshape(1, QL), "kv_a_norm": wts["kv_a_norm"].reshape(1, KVL),
         "q_gain": _gain_slab(wts["q_norm_nope"].reshape(1, NOPE), wts["q_norm_rope"].reshape(1, ROPE)),
         "k_gain": _gain_slab(wts["k_norm_nope"].reshape(1, NOPE), wts["k_norm_rope"].reshape(1, ROPE))}
    cos, sin = _rope_tables(positions.reshape(tokens))

    loss8, grad_x, G, small, dmod = _layer_fwd_bwd(x.reshape(tokens, dm), loss_target.reshape(tokens, dm), mod3, cos, sin, W, P)
    loss = lax.psum(loss8[0, 0], ("x", "y", "c"))

    gfull = {KERNEL_NAME[n]: G[KERNEL_NAME[n]] for n in BIG}
    gfull["w_in"] = _w_in_from_kernel(G["w_in"])
    gfull["q_up"] = _q_up_from_kernel(G["q_up"])
    gfull["kv_up"] = _kv_up_from_kernel(G["kv_up"])
    parts = [G[KERNEL_NAME[n]] if n in GU else _split(n, gfull[KERNEL_NAME[n]]) for n in BIG]
    cidx = ci.astype(jnp.int32).reshape(1)
    from_sib = _swap_halves(parts, name="rs_swap_halves")
    pre = [_add_half(p, o, cidx, name="rs_add_" + KERNEL_NAME[n]) for n, p, o in zip(BIG, parts, from_sib)]
    landed = _exchange_chips(pre, name="rs_exchange_chips")
    halves = [_sum_chips(q, cidx, name="rs_sum_" + KERNEL_NAME[n]) for n, q in zip(BIG, landed)]
    joined = _join_halves(halves, name="rs_join_halves")
    grads = {n: g for n, g in zip(BIG, joined)}

    qg, kg = small["q_gain"], small["k_gain"]
    per_seq = {"b_ada": dmod.reshape(nb, 9 * dm), "norm_ffn1": small["norm_ffn1"], "norm_mix": small["norm_mix"],
               "pool_scale": small["pool_scale"], "q_a_norm": small["q_a_norm"],
               "kv_a_norm": small["kv_a_norm"], "q_norm_nope": qg[:, :NOPE], "q_norm_rope": qg[:, NOPE:NOPE + ROPE],
               "k_norm_nope": kg[:, :NOPE], "k_norm_rope": kg[:, KR_LANE:KR_LANE + ROPE], "norm_ffn2": small["norm_ffn2"]}
    slabs, spans = [], None
    for s in range(nb):
        slab, spans = _pack_small({n: per_seq[n][s] for n in SMALL_SEQ}, SMALL_SEQ)
        slabs.append(slab)
    rows = slabs[0].shape[0]
    slabs.append(_to_rows(small["pool_grp"]))
    gathered_small = _gather8(jnp.concatenate(slabs, axis=0), name="gather_small")
    small_sum = _sum_small(gathered_small, nb, rows, name="sum_small")
    small_grads = _unpack_small(small_sum[:rows], spans)
    small_grads["pool_grp"] = small_sum[rows:]
    for n in SMALL:
        grads[n] = small_grads[n].reshape(wts[n].shape)
    at, _, _ = spans["b_ada"]
    nrow_b = 9 * dm // SLAB_W
    dmod_all = gathered_small[:, :nb * rows].reshape(N_DEV * nb, rows, SLAB_W)[:, at:at + nrow_b].reshape(N_DEV * nb, 9 * dm)
    dmod_cols = lax.dynamic_slice_in_dim(dmod_all, chip * ncol, ncol, axis=1)
    grads["w_ada"] = _ada_bwd(c_all, dmod_cols, name="ada_bwd")

    delta, new_m, new_v = {}, {}, {}
    for n in ("w_ada",) + BIG:
        delta[n], new_m[n], new_v[n] = _adamw(wts[n], grads[n], mom[n], var[n], name="adamw_" + n)
    w_slab, sp = _pack_small({n: wts[n] for n in SMALL})
    g_slab, _ = _pack_small({n: grads[n] for n in SMALL})
    m_slab, _ = _pack_small({n: mom[n] for n in SMALL})
    v_slab, _ = _pack_small({n: var[n] for n in SMALL})
    d_s, m_s, v_s = _adamw(w_slab, g_slab, m_slab, v_slab, name="adamw_small")
    for dst, slab in ((delta, d_s), (new_m, m_s), (new_v, v_s)):
        un = _unpack_small(slab, sp)
        for n in SMALL:
            dst[n] = un[n]

    def lead(a, n):
        return a.reshape((1,) + wts[n].shape)

    return (loss, grad_x.reshape(nb, seq, dm), *[lead(grads[n], n) for n in WEIGHTS], *[lead(delta[n], n) for n in WEIGHTS],
            *[lead(new_m[n], n) for n in WEIGHTS], *[lead(new_v[n], n) for n in WEIGHTS])
```

```python
import functools
import math

import jax
import jax.numpy as jnp
from jax import lax
from jax.experimental import pallas as pl
from jax.experimental.pallas import tpu as pltpu

F32 = jnp.float32
MX = jnp.bfloat16

D = 1024
DFF = 2816
NH = 8
POOL_W = 512
POOL_G = 4
QL = 384
KVL = 256
ROPE = 32
NOPE = 64
LANE = 128
SUBLANE = 8
EPS = 1e-6
ATTN_SCALE = 1.0 / math.sqrt(96.0)
NEG = -1e30

Z_UP, Z_QL, Z_KV, Z_KR, Z_GP, Z_GM, Z_W = 0, 512, 896, 1152, 1280, 2304, 3328
KR_LANE = 64
LAT_W = 1280

ADAM_LR, ADAM_B1, ADAM_B2, ADAM_EPS, ADAM_WD, ADAM_STEP = 0.001, 0.9, 0.999, 1e-08, 0.01, 10

VMEM_LIMIT = 48 * 1024 * 1024
MESH = pl.DeviceIdType.MESH
N_DEV = 8
N_CHIP = 4


class _Comm:
    def __init__(self, ins, out_shapes, sems, start, finish, aliases=None):
        self.ins, self.out_shapes, self.sems = list(ins), list(out_shapes), list(sems)
        self.start, self.finish = start, finish
        self.aliases = aliases or {}


def _pcall(body, *, name, out_shape, grid=(), in_specs=None, out_specs=None, scratch=(), grid_spec=None, aliases=None,
           comm=None):
    params = pltpu.CompilerParams(vmem_limit_bytes=VMEM_LIMIT)
    kw = dict(name=name, compiler_params=params)
    if comm is None:
        if aliases:
            kw["input_output_aliases"] = aliases
        if grid_spec is not None:
            return pl.pallas_call(body, grid_spec=grid_spec, out_shape=out_shape, **kw)
        return pl.pallas_call(body, grid=grid, in_specs=in_specs, out_specs=out_specs, scratch_shapes=scratch,
                              out_shape=out_shape, **kw)
    single = not isinstance(out_shape, (list, tuple))
    outs = [out_shape] if single else list(out_shape)
    ospecs = [out_specs] if single else list(out_specs)
    n_in, n_out, n_ci, n_co, n_scr = len(in_specs), len(outs), len(comm.ins), len(comm.out_shapes), len(scratch)
    io = dict(aliases or {})
    io.update({n_in + i: n_out + o for i, o in comm.aliases.items()})

    def riding(*refs):
        ins, cins = refs[:n_in], refs[n_in:n_in + n_ci]
        at = n_in + n_ci
        os_, couts = refs[at:at + n_out], refs[at + n_out:at + n_out + n_co]
        at += n_out + n_co
        scr, csems = refs[at:at + n_scr], refs[at + n_scr:]
        if grid:
            first = functools.reduce(jnp.logical_and, [pl.program_id(d) == 0 for d in range(len(grid))])
            last = functools.reduce(jnp.logical_and, [pl.program_id(d) == grid[d] - 1 for d in range(len(grid))])
            pl.when(first)(lambda: comm.start(cins, couts, csems))
            body(*ins, *os_, *scr)
            pl.when(last)(lambda: comm.finish(cins, couts, csems))
        else:
            comm.start(cins, couts, csems)
            body(*ins, *os_, *scr)
            comm.finish(cins, couts, csems)

    call = pl.pallas_call(riding, grid=grid, in_specs=list(in_specs) + [HBM_SPEC] * n_ci, out_specs=ospecs + [HBM_SPEC] * n_co,
                          out_shape=outs + comm.out_shapes, scratch_shapes=list(scratch) + comm.sems,
                          input_output_aliases=io, **kw)

    def run(*args):
        res = call(*args, *comm.ins)
        main = list(res[:n_out])
        return (main[0] if single else main), list(res[n_out:])

    return run


def _pick(dim, target):
    best = None
    for t in range(LANE, min(dim, target) + 1, LANE):
        if dim % t == 0:
            best = t
    return dim if best is None else best


def _sds(shape, dtype):
    return jax.ShapeDtypeStruct(shape, dtype)


def _mm(a, b, *, mode, out_dtype, name, tm=512, tn=512, tk=1024, n_outer=False, comm=None):
    if mode == "nn":
        (M, K), (K2, N) = a.shape, b.shape
    elif mode == "nt":
        (M, K), (N, K2) = a.shape, b.shape
    else:
        (K, M), (K2, N) = a.shape, b.shape
    assert K == K2, (name, a.shape, b.shape)
    tm, tn, tk = _pick(M, tm), _pick(N, tn), _pick(K, tk)
    nk = K // tk
    if n_outer:
        ij = lambda g0, g1: (g1, g0)
        grid = (N // tn, M // tm, nk)
    else:
        ij = lambda g0, g1: (g0, g1)
        grid = (M // tm, N // tn, nk)
    if mode == "tn":
        a_spec = pl.BlockSpec((tk, tm), lambda g0, g1, k: (k, ij(g0, g1)[0]))
    else:
        a_spec = pl.BlockSpec((tm, tk), lambda g0, g1, k: (ij(g0, g1)[0], k))
    if mode == "nt":
        b_spec = pl.BlockSpec((tn, tk), lambda g0, g1, k: (ij(g0, g1)[1], k))
    else:
        b_spec = pl.BlockSpec((tk, tn), lambda g0, g1, k: (k, ij(g0, g1)[1]))
    o_spec = pl.BlockSpec((tm, tn), lambda g0, g1, k: ij(g0, g1))
    dn = {"nn": (((1,), (0,)), ((), ())), "nt": (((1,), (1,)), ((), ())), "tn": (((0,), (0,)), ((), ()))}[mode]

    def dot(a_ref, b_ref):
        return lax.dot_general(a_ref[...].astype(MX), b_ref[...].astype(MX), dn, preferred_element_type=F32)

    def body_one(a_ref, b_ref, o_ref):
        o_ref[...] = dot(a_ref, b_ref).astype(o_ref.dtype)

    def body_acc(a_ref, b_ref, o_ref, acc_ref):
        k = pl.program_id(2)
        part = dot(a_ref, b_ref)

        @pl.when(k == 0)
        def _():
            acc_ref[...] = part

        @pl.when(k > 0)
        def _():
            acc_ref[...] += part

        @pl.when(k == nk - 1)
        def _():
            o_ref[...] = acc_ref[...].astype(o_ref.dtype)

    return _pcall(body_one if nk == 1 else body_acc, name=name, out_shape=_sds((M, N), out_dtype), grid=grid,
                  in_specs=[a_spec, b_spec], out_specs=o_spec, scratch=[] if nk == 1 else [pltpu.VMEM((tm, tn), F32)],
                  comm=comm)(a, b)


def _gu_shard(q):
    return (q % 2) * 2 + q // 2


def _ffn_up(h, w_st, *, name, tm=512, comm=None):
    T, dm = h.shape
    hw = w_st.shape[2]
    tm = _pick(T, tm)

    def body(h_ref, wg_ref, wu_ref, gu_ref, a_ref):
        hv = h_ref[...]
        g = jnp.dot(hv, wg_ref[0], preferred_element_type=F32)
        u = jnp.dot(hv, wu_ref[0], preferred_element_type=F32)
        gu_ref[:, :hw] = g.astype(gu_ref.dtype)
        gu_ref[:, hw:] = u.astype(gu_ref.dtype)
        a_ref[...] = (g * _sigmoid(g) * u).astype(a_ref.dtype)

    return _pcall(body, name=name, grid=(T // tm, 2),
                  in_specs=[pl.BlockSpec((tm, dm), lambda i, j: (i, 0)), pl.BlockSpec((1, dm, hw), lambda i, j: (j, 0, 0)),
                            pl.BlockSpec((1, dm, hw), lambda i, j: (2 + j, 0, 0))],
                  out_specs=[pl.BlockSpec((tm, 2 * hw), lambda i, j: (i, j)), pl.BlockSpec((tm, hw), lambda i, j: (i, j))],
                  out_shape=[_sds((T, 4 * hw), MX), _sds((T, 2 * hw), MX)], comm=comm)(h, w_st, w_st)


def _ffn_dact(df, gu, w_out, *, name, tm=512, comm=None):
    T, dm = df.shape
    hw = gu.shape[1] // 4
    tm = _pick(T, tm)

    def body(df_ref, gu_ref, wo_ref, dgu_ref):
        da = lax.dot_general(df_ref[...], wo_ref[...], (((1,), (1,)), ((), ())), preferred_element_type=F32)
        g = gu_ref[:, :hw].astype(F32)
        u = gu_ref[:, hw:].astype(F32)
        s = _sigmoid(g)
        dgu_ref[:, :hw] = (da * u * (s * (1.0 + g * (1.0 - s)))).astype(dgu_ref.dtype)
        dgu_ref[:, hw:] = (da * (g * s)).astype(dgu_ref.dtype)

    return _pcall(body, name=name, grid=(T // tm, 2),
                  in_specs=[pl.BlockSpec((tm, dm), lambda i, j: (i, 0)), pl.BlockSpec((tm, 2 * hw), lambda i, j: (i, j)),
                            pl.BlockSpec((hw, dm), lambda i, j: (j, 0))],
                  out_specs=pl.BlockSpec((tm, 2 * hw), lambda i, j: (i, j)), out_shape=_sds(gu.shape, MX),
                  comm=comm)(df, gu, w_out)


def _ffn_dh(dgu, w_st, *, name, tm=1024, comm=None):
    T = dgu.shape[0]
    _, dm, hw = w_st.shape
    tm = _pick(T, tm)

    def body(d_ref, w_ref, o_ref, acc_ref):
        q = pl.program_id(1)
        part = lax.dot_general(d_ref[...], w_ref[0], (((1,), (1,)), ((), ())), preferred_element_type=F32)

        @pl.when(q == 0)
        def _():
            acc_ref[...] = part

        @pl.when(q > 0)
        def _():
            acc_ref[...] += part

        @pl.when(q == 3)
        def _():
            o_ref[...] = acc_ref[...]

    return _pcall(body, name=name, grid=(T // tm, 4),
                  in_specs=[pl.BlockSpec((tm, hw), lambda i, q: (i, q)), pl.BlockSpec((1, dm, hw), lambda i, q: (_gu_shard(q), 0, 0))],
                  out_specs=pl.BlockSpec((tm, dm), lambda i, q: (i, 0)), out_shape=_sds((T, dm), F32),
                  scratch=[pltpu.VMEM((tm, dm), F32)], comm=comm)(dgu, w_st)


def _ffn_dw_in(h, dgu, *, name, tm=256, comm=None):
    T, dm = h.shape
    hw = dgu.shape[1] // 4
    tm = _pick(dm, tm)

    def body(h_ref, d_ref, o_ref):
        o_ref[0] = lax.dot_general(h_ref[...], d_ref[...], (((0,), (0,)), ((), ())), preferred_element_type=F32)

    return _pcall(body, name=name, grid=(4, dm // tm),
                  in_specs=[pl.BlockSpec((T, tm), lambda q, i: (0, i)), pl.BlockSpec((T, hw), lambda q, i: (0, q))],
                  out_specs=pl.BlockSpec((1, tm, hw), lambda q, i: (_gu_shard(q), i, 0)),
                  out_shape=_sds((4, dm, hw), F32), comm=comm)(h, dgu)


def _rsq(x):
    return lax.rsqrt(jnp.mean(x * x, axis=-1, keepdims=True) + EPS)


def _sigmoid(x):
    return 1.0 / (1.0 + jnp.exp(-x))


def _row_spec(tm, w):
    return pl.BlockSpec((tm, w), lambda i: (i, 0))


def _fwd_block(x_prev, f_prev, mod3, gain, *, sub, coef, name, tm=256):
    T, dm = x_prev.shape
    tps = (T // mod3.shape[0]) // tm
    has_f = f_prev is not None
    mod_spec = pl.BlockSpec((1, 9, dm), lambda i: (i // tps, 0, 0))
    vec_spec = pl.BlockSpec((1, dm), lambda i: (0, 0))

    def body(*refs):
        if has_f:
            x_ref, f_ref, mod_ref, n_ref, xo_ref, h_ref = refs
            x = x_ref[...] + coef * mod_ref[0, 3 * sub - 1:3 * sub, :] * f_ref[...]
            xo_ref[...] = x
        else:
            x_ref, mod_ref, n_ref, h_ref = refs
            x = x_ref[...]
        xn = x * _rsq(x) * n_ref[...]
        h = xn * (1.0 + mod_ref[0, 3 * sub + 1:3 * sub + 2, :]) + mod_ref[0, 3 * sub:3 * sub + 1, :]
        h_ref[...] = h.astype(h_ref.dtype)

    row = _row_spec(tm, dm)
    if has_f:
        return _pcall(body, name=name, grid=(T // tm,), in_specs=[row, row, mod_spec, vec_spec], out_specs=[row, row],
                      out_shape=[_sds((T, dm), F32), _sds((T, dm), MX)])(x_prev, f_prev, mod3, gain)
    return _pcall(body, name=name, grid=(T // tm,), in_specs=[row, mod_spec, vec_spec], out_specs=row,
                  out_shape=_sds((T, dm), MX))(x_prev, mod3, gain)


def _final(x2, f2, tgt, mod3, *, name, tm=256):
    T, dm = x2.shape
    tps = (T // mod3.shape[0]) // tm
    mod_spec = pl.BlockSpec((1, 9, dm), lambda i: (i // tps, 0, 0))
    row = _row_spec(tm, dm)
    stat_spec = pl.BlockSpec((1, SUBLANE, dm), lambda i: (i // tps, 0, 0))
    loss_spec = pl.BlockSpec((SUBLANE, LANE), lambda i: (0, 0))

    def body(x_ref, f_ref, t_ref, mod_ref, dy_ref, df_ref, st_ref, loss_ref):
        i = pl.program_id(0)
        g = mod_ref[0, 8:9, :]
        f = f_ref[...]
        err = x_ref[...] + 0.5 * g * f - t_ref[...]
        dy = err * (1.0 / dm)
        dy_ref[...] = dy
        df_ref[...] = (0.5 * g * dy).astype(df_ref.dtype)
        dgate = jnp.sum(0.5 * dy * f, axis=0, keepdims=True)
        part = 0.5 * jnp.sum(jnp.sum(err * err, axis=0, keepdims=True), axis=1, keepdims=True) * (1.0 / dm)

        @pl.when(i % tps == 0)
        def _():
            st_ref[...] = jnp.zeros_like(st_ref)

        @pl.when(i == 0)
        def _():
            loss_ref[...] = jnp.zeros_like(loss_ref)

        st_ref[0, 0:1, :] += dgate
        loss_ref[...] += jnp.broadcast_to(part, loss_ref.shape)

    return _pcall(body, name=name, grid=(T // tm,), in_specs=[row, row, row, mod_spec],
                  out_specs=[row, row, stat_spec, loss_spec],
                  out_shape=[_sds((T, dm), F32), _sds((T, dm), MX), _sds((mod3.shape[0], SUBLANE, dm), F32),
                             _sds((SUBLANE, LANE), F32)])(x2, f2, tgt, mod3)


def _bwd_block(x_cur, dh, dx_in, f_prev, mod3, gain, *, sub, coef, name, tm=256, comm=None):
    T, dm = x_cur.shape
    nb = mod3.shape[0]
    tps = (T // nb) // tm
    has_f = f_prev is not None
    mod_spec = pl.BlockSpec((1, 9, dm), lambda i: (i // tps, 0, 0))
    vec_spec = pl.BlockSpec((1, dm), lambda i: (0, 0))
    stat_spec = pl.BlockSpec((1, SUBLANE, dm), lambda i: (i // tps, 0, 0))
    row = _row_spec(tm, dm)

    def body(*refs):
        if has_f:
            x_ref, dh_ref, dxi_ref, f_ref, mod_ref, n_ref, dx_ref, df_ref, st_ref = refs
        else:
            x_ref, dh_ref, dxi_ref, mod_ref, n_ref, dx_ref, st_ref = refs
        i = pl.program_id(0)
        x = x_ref[...]
        r = _rsq(x)
        xhat = x * r
        n = n_ref[...]
        dhv = dh_ref[...]
        d_shift = jnp.sum(dhv, axis=0, keepdims=True)
        d_scale = jnp.sum(dhv * (xhat * n), axis=0, keepdims=True)
        dxn = dhv * (1.0 + mod_ref[0, 3 * sub + 1:3 * sub + 2, :])
        d_gain = jnp.sum(dxn * xhat, axis=0, keepdims=True)
        dxhat = dxn * n
        dx = dxi_ref[...] + r * (dxhat - xhat * jnp.mean(dxhat * xhat, axis=-1, keepdims=True))
        dx_ref[...] = dx

        @pl.when(i % tps == 0)
        def _():
            st_ref[...] = jnp.zeros_like(st_ref)

        st_ref[0, 0:1, :] += d_shift
        st_ref[0, 1:2, :] += d_scale
        st_ref[0, 2:3, :] += d_gain
        if has_f:
            f = f_ref[...]
            st_ref[0, 3:4, :] += jnp.sum(coef * dx * f, axis=0, keepdims=True)
            df_ref[...] = (coef * mod_ref[0, 3 * sub - 1:3 * sub, :] * dx).astype(df_ref.dtype)

    st_shape = _sds((nb, SUBLANE, dm), F32)
    if has_f:
        return _pcall(body, name=name, grid=(T // tm,), in_specs=[row, row, row, row, mod_spec, vec_spec],
                      out_specs=[row, row, stat_spec],
                      out_shape=[_sds((T, dm), F32), _sds((T, dm), MX), st_shape], comm=comm)(x_cur, dh, dx_in, f_prev, mod3, gain)
    return _pcall(body, name=name, grid=(T // tm,), in_specs=[row, row, row, mod_spec, vec_spec],
                  out_specs=[row, stat_spec], out_shape=[_sds((T, dm), F32), st_shape], comm=comm)(x_cur, dh, dx_in, mod3, gain)


def _merge_fwd(z, br_pool, br_mla, *, name, tm=256):
    T = z.shape[0]

    def body(z_ref, bp_ref, bm_ref, o_ref):
        gp = z_ref[:, Z_GP:Z_GP + D]
        gm = z_ref[:, Z_GM:Z_GM + D]
        o_ref[...] = (_sigmoid(gp) * bp_ref[...] + _sigmoid(gm) * bm_ref[...]).astype(o_ref.dtype)

    return _pcall(body, name=name, grid=(T // tm,), in_specs=[_row_spec(tm, Z_W), _row_spec(tm, D), _row_spec(tm, D)],
                  out_specs=_row_spec(tm, D), out_shape=_sds((T, D), MX))(z, br_pool, br_mla)


def _merge_bwd(z, br_pool, br_mla, dmerged, *, name, tm=256):
    T = z.shape[0]

    def body(z_ref, bp_ref, bm_ref, dm_ref, dbp_ref, dbm_ref, dg_ref):
        dm = dm_ref[...]
        sp = _sigmoid(z_ref[:, Z_GP:Z_GP + D])
        sm = _sigmoid(z_ref[:, Z_GM:Z_GM + D])
        dbp_ref[...] = (dm * sp).astype(dbp_ref.dtype)
        dbm_ref[...] = (dm * sm).astype(dbm_ref.dtype)
        dg_ref[:, :D] = (dm * bp_ref[...] * sp * (1.0 - sp)).astype(dg_ref.dtype)
        dg_ref[:, D:] = (dm * bm_ref[...] * sm * (1.0 - sm)).astype(dg_ref.dtype)

    return _pcall(body, name=name, grid=(T // tm,),
                  in_specs=[_row_spec(tm, Z_W), _row_spec(tm, D), _row_spec(tm, D), _row_spec(tm, D)],
                  out_specs=[_row_spec(tm, D), _row_spec(tm, D), _row_spec(tm, 2 * D)],
                  out_shape=[_sds((T, D), MX), _sds((T, D), MX), _sds((T, 2 * D), MX)])(z, br_pool, br_mla, dmerged)


def _shift_down(x, k, row):
    return jnp.where(row >= k, pltpu.roll(x, k, 0), 0.0)


def _shift_up(x, k, row, n):
    return jnp.where(row < n - k, pltpu.roll(x, n - k, 0), 0.0)


def _pool_fwd(z, pool_grp, pool_scale, *, nb, name):
    T = z.shape[0]
    S = T // nb
    blk = pl.BlockSpec((S, LANE), lambda b, g: (b, g))

    def body(u_ref, w_ref, s_ref, pooled_ref, mixed_ref, scaled_ref):
        g = pl.program_id(1)
        u = u_ref[...]
        row = lax.broadcasted_iota(jnp.int32, u.shape, 0)
        s2 = u + _shift_down(u, 1, row)
        s4 = s2 + _shift_down(s2, 2, row)
        s8 = s4 + _shift_down(s4, 4, row)
        s16 = s8 + _shift_down(s8, 8, row)
        win = jnp.where(g == 0, s2, jnp.where(g == 1, s4, jnp.where(g == 2, s8, s16)))
        width = lax.shift_left(jnp.int32(2), g)
        cnt = jnp.minimum(row + 1, width).astype(F32)
        pooled = (win / cnt - u).astype(MX)
        pooled_ref[...] = pooled
        mixed = jnp.dot(pooled, w_ref[0], preferred_element_type=F32)
        mixed_ref[...] = mixed
        scaled_ref[...] = (mixed * s_ref[...]).astype(scaled_ref.dtype)

    return _pcall(body, name=name, grid=(nb, POOL_G),
                  in_specs=[blk, pl.BlockSpec((1, LANE, LANE), lambda b, g: (g, 0, 0)),
                            pl.BlockSpec((1, LANE), lambda b, g: (0, g))],
                  out_specs=[blk, blk, blk],
                  out_shape=[_sds((T, POOL_W), MX), _sds((T, POOL_W), F32), _sds((T, POOL_W), MX)])(z, pool_grp, pool_scale)


def _pool_bwd(dscaled, mixed, pooled, pool_grp, pool_scale, *, nb, name):
    T = dscaled.shape[0]
    S = T // nb
    blk = pl.BlockSpec((S, LANE), lambda g, b: (b, g))

    def body(ds_ref, mixed_ref, pooled_ref, w_ref, s_ref, du_ref, dw_ref, dsc_ref):
        g = pl.program_id(0)
        b = pl.program_id(1)
        ds = ds_ref[...]
        dsc_ref[0] = jnp.sum(ds * mixed_ref[...], axis=0, keepdims=True)
        dmixed = (ds * s_ref[...]).astype(MX)
        dw = lax.dot_general(pooled_ref[...], dmixed, (((0,), (0,)), ((), ())), preferred_element_type=F32)

        @pl.when(b == 0)
        def _():
            dw_ref[0] = dw

        @pl.when(b > 0)
        def _():
            dw_ref[0] += dw
        dpooled = lax.dot_general(dmixed, w_ref[0], (((1,), (1,)), ((), ())), preferred_element_type=F32)
        row = lax.broadcasted_iota(jnp.int32, dpooled.shape, 0)
        width = lax.shift_left(jnp.int32(2), g)
        q = dpooled / jnp.minimum(row + 1, width).astype(F32)
        r2 = q + _shift_up(q, 1, row, S)
        r4 = r2 + _shift_up(r2, 2, row, S)
        r8 = r4 + _shift_up(r4, 4, row, S)
        r16 = r8 + _shift_up(r8, 8, row, S)
        win = jnp.where(g == 0, r2, jnp.where(g == 1, r4, jnp.where(g == 2, r8, r16)))
        du_ref[...] = (win - dpooled).astype(du_ref.dtype)

    return _pcall(body, name=name, grid=(POOL_G, nb),
                  in_specs=[blk, blk, blk, pl.BlockSpec((1, LANE, LANE), lambda g, b: (g, 0, 0)),
                            pl.BlockSpec((1, LANE), lambda g, b: (0, g))],
                  out_specs=[blk, pl.BlockSpec((1, LANE, LANE), lambda g, b: (g, 0, 0)),
                             pl.BlockSpec((1, 1, LANE), lambda g, b: (b, 0, g))],
                  out_shape=[_sds((T, POOL_W), MX), _sds((POOL_G, LANE, LANE), F32), _sds((nb, 1, POOL_W), F32)],
                  )(dscaled, mixed, pooled, pool_grp, pool_scale)


def _lat_fwd(z, q_gain, kv_gain, *, name, tm=256):
    T = z.shape[0]

    def body(z_ref, qg_ref, kg_ref, qn_ref, kvn_ref):
        ql = z_ref[:, Z_QL:Z_QL + QL]
        kv = z_ref[:, Z_KV:Z_KV + KVL]
        qn_ref[...] = (ql * _rsq(ql) * qg_ref[...]).astype(qn_ref.dtype)
        kvn_ref[...] = (kv * _rsq(kv) * kg_ref[...]).astype(kvn_ref.dtype)

    return _pcall(body, name=name, grid=(T // tm,),
                  in_specs=[_row_spec(tm, LAT_W), pl.BlockSpec((1, QL), lambda i: (0, 0)), pl.BlockSpec((1, KVL), lambda i: (0, 0))],
                  out_specs=[_row_spec(tm, QL), _row_spec(tm, KVL)],
                  out_shape=[_sds((T, QL), MX), _sds((T, KVL), MX)])(z, q_gain, kv_gain)


def _lat_bwd(z, dqn, dkvn, q_gain, kv_gain, *, nb, name, tm=256):
    T = z.shape[0]
    tps = (T // nb) // tm

    def norm_bwd(x, dy, gain):
        r = _rsq(x)
        xhat = x * r
        dgain = jnp.sum(dy * xhat, axis=0, keepdims=True)
        dxhat = dy * gain
        return r * (dxhat - xhat * jnp.mean(dxhat * xhat, axis=-1, keepdims=True)), dgain

    def body(z_ref, dq_ref, dkv_ref, qg_ref, kg_ref, dql_ref, dkvl_ref, sq_ref, sk_ref):
        i = pl.program_id(0)
        dql, dqg = norm_bwd(z_ref[:, Z_QL:Z_QL + QL], dq_ref[...], qg_ref[...])
        dkvl, dkg = norm_bwd(z_ref[:, Z_KV:Z_KV + KVL], dkv_ref[...], kg_ref[...])
        dql_ref[...] = dql.astype(dql_ref.dtype)
        dkvl_ref[...] = dkvl.astype(dkvl_ref.dtype)

        @pl.when(i % tps == 0)
        def _():
            sq_ref[...] = jnp.zeros_like(sq_ref)
            sk_ref[...] = jnp.zeros_like(sk_ref)

        sq_ref[0, 0:1, :] += dqg
        sk_ref[0, 0:1, :] += dkg

    return _pcall(body, name=name, grid=(T // tm,),
                  in_specs=[_row_spec(tm, LAT_W), _row_spec(tm, QL), _row_spec(tm, KVL),
                            pl.BlockSpec((1, QL), lambda i: (0, 0)), pl.BlockSpec((1, KVL), lambda i: (0, 0))],
                  out_specs=[_row_spec(tm, QL), _row_spec(tm, KVL),
                             pl.BlockSpec((1, SUBLANE, QL), lambda i: (i // tps, 0, 0)),
                             pl.BlockSpec((1, SUBLANE, KVL), lambda i: (i // tps, 0, 0))],
                  out_shape=[_sds((T, QL), MX), _sds((T, KVL), MX), _sds((nb, SUBLANE, QL), F32), _sds((nb, SUBLANE, KVL), F32)],
                  )(z, dqn, dkvn, q_gain, kv_gain)


def _lane_masks(shape):
    lane = lax.broadcasted_iota(jnp.int32, shape, len(shape) - 1)
    m_n = lane < NOPE
    m_r = jnp.logical_and(lane >= KR_LANE, lane < KR_LANE + ROPE)
    first_half = lane < KR_LANE + ROPE // 2
    return m_n, m_r, first_half


def _rot(y, first_half):
    return jnp.where(first_half, -pltpu.roll(y, LANE - ROPE // 2, 1), pltpu.roll(y, ROPE // 2, 1))


def _rot_t(v, first_half, m_r):
    return jnp.where(m_r, jnp.where(first_half, pltpu.roll(v, LANE - ROPE // 2, 1), -pltpu.roll(v, ROPE // 2, 1)), 0.0)


def _prep_fwd(qp, kvp, z, cos, sin, q_gain, k_gain, *, name, tm=256):
    T = qp.shape[0]
    slab = pl.BlockSpec((tm, LANE), lambda i: (i, 0))
    kr_spec = pl.BlockSpec((tm, LANE), lambda i: (i, Z_KR // LANE))
    vec = pl.BlockSpec((1, LANE), lambda i: (0, 0))

    def body(qp_ref, kvp_ref, kr_ref, cos_ref, sin_ref, qg_ref, kg_ref, q_ref, k_ref, v_ref):
        m_n, m_r, first_half = _lane_masks((tm, LANE))
        c = cos_ref[...]
        s = sin_ref[...]
        qg = qg_ref[...]
        kg = kg_ref[...]
        xr = kr_ref[...]
        rr = lax.rsqrt(jnp.sum(xr * xr, axis=-1, keepdims=True) * (1.0 / ROPE) + EPS)
        yr = xr * rr * kg
        kr = jnp.where(m_r, yr * c + _rot(yr, first_half) * s, 0.0)
        for h in range(NH):
            x = qp_ref[:, h * LANE:(h + 1) * LANE]
            x2 = x * x
            rn = lax.rsqrt(jnp.sum(jnp.where(m_n, x2, 0.0), axis=-1, keepdims=True) * (1.0 / NOPE) + EPS)
            rq = lax.rsqrt(jnp.sum(jnp.where(m_r, x2, 0.0), axis=-1, keepdims=True) * (1.0 / ROPE) + EPS)
            y = x * jnp.where(m_n, rn, jnp.where(m_r, rq, 0.0)) * qg
            q_ref[:, h * LANE:(h + 1) * LANE] = (y * c + _rot(y, first_half) * s).astype(q_ref.dtype)
            xk = kvp_ref[:, h * LANE:(h + 1) * LANE]
            rk = lax.rsqrt(jnp.sum(jnp.where(m_n, xk * xk, 0.0), axis=-1, keepdims=True) * (1.0 / NOPE) + EPS)
            k_ref[:, h * LANE:(h + 1) * LANE] = (jnp.where(m_n, xk * rk * kg, 0.0) + kr).astype(k_ref.dtype)
        v_ref[...] = kvp_ref[:, NH * LANE:].astype(v_ref.dtype)

    return _pcall(body, name=name, grid=(T // tm,),
                  in_specs=[_row_spec(tm, NH * LANE), _row_spec(tm, NH * LANE + NH * NOPE), kr_spec, slab, slab, vec, vec],
                  out_specs=[_row_spec(tm, NH * LANE), _row_spec(tm, NH * LANE), _row_spec(tm, NH * NOPE)],
                  out_shape=[_sds((T, NH * LANE), MX), _sds((T, NH * LANE), MX), _sds((T, NH * NOPE), MX)],
                  )(qp, kvp, z, cos, sin, q_gain, k_gain)


def _prep_bwd(dq, dk, dv, qp, kvp, z, cos, sin, q_gain, k_gain, *, nb, name, tm=256):
    T = qp.shape[0]
    tps = (T // nb) // tm
    slab = pl.BlockSpec((tm, LANE), lambda i: (i, 0))
    kr_spec = pl.BlockSpec((tm, LANE), lambda i: (i, Z_KR // LANE))
    vec = pl.BlockSpec((1, LANE), lambda i: (0, 0))

    def body(dq_ref, dk_ref, dv_ref, qp_ref, kvp_ref, kr_ref, cos_ref, sin_ref, qg_ref, kg_ref,
             dqp_ref, dkvp_ref, dkr_ref, st_ref):
        i = pl.program_id(0)
        m_n, m_r, first_half = _lane_masks((tm, LANE))
        c = cos_ref[...]
        s = sin_ref[...]
        qg = qg_ref[...]
        kg = kg_ref[...]
        dqg = jnp.zeros((1, LANE), F32)
        dkg = jnp.zeros((1, LANE), F32)
        dkr_sum = jnp.zeros((tm, LANE), F32)
        for h in range(NH):
            x = qp_ref[:, h * LANE:(h + 1) * LANE]
            x2 = x * x
            rn = lax.rsqrt(jnp.sum(jnp.where(m_n, x2, 0.0), axis=-1, keepdims=True) * (1.0 / NOPE) + EPS)
            rq = lax.rsqrt(jnp.sum(jnp.where(m_r, x2, 0.0), axis=-1, keepdims=True) * (1.0 / ROPE) + EPS)
            rfac = jnp.where(m_n, rn, jnp.where(m_r, rq, 0.0))
            xhat = x * rfac
            do = dq_ref[:, h * LANE:(h + 1) * LANE]
            dy = do * c + _rot_t(do * s, first_half, m_r)
            dqg = dqg + jnp.sum(dy * xhat, axis=0, keepdims=True)
            dxhat = dy * qg
            t = dxhat * xhat
            mean_n = jnp.sum(jnp.where(m_n, t, 0.0), axis=-1, keepdims=True) * (1.0 / NOPE)
            mean_r = jnp.sum(jnp.where(m_r, t, 0.0), axis=-1, keepdims=True) * (1.0 / ROPE)
            dqp_ref[:, h * LANE:(h + 1) * LANE] = (
                rfac * (dxhat - xhat * jnp.where(m_n, mean_n, jnp.where(m_r, mean_r, 0.0)))).astype(dqp_ref.dtype)

            xk = kvp_ref[:, h * LANE:(h + 1) * LANE]
            rk = lax.rsqrt(jnp.sum(jnp.where(m_n, xk * xk, 0.0), axis=-1, keepdims=True) * (1.0 / NOPE) + EPS)
            khat = jnp.where(m_n, xk * rk, 0.0)
            dko = dk_ref[:, h * LANE:(h + 1) * LANE]
            dkn = jnp.where(m_n, dko, 0.0)
            dkg = dkg + jnp.sum(dkn * khat, axis=0, keepdims=True)
            dkhat = dkn * kg
            mean_k = jnp.sum(dkhat * khat, axis=-1, keepdims=True) * (1.0 / NOPE)
            dkvp_ref[:, h * LANE:(h + 1) * LANE] = jnp.where(m_n, rk * (dkhat - khat * mean_k), 0.0).astype(dkvp_ref.dtype)
            dkr_sum = dkr_sum + jnp.where(m_r, dko, 0.0)
        dkvp_ref[:, NH * LANE:] = dv_ref[...].astype(dkvp_ref.dtype)

        xr = kr_ref[...]
        rr = lax.rsqrt(jnp.sum(xr * xr, axis=-1, keepdims=True) * (1.0 / ROPE) + EPS)
        rhat = xr * rr
        dyr = dkr_sum * c + _rot_t(dkr_sum * s, first_half, m_r)
        dkg = dkg + jnp.sum(dyr * rhat, axis=0, keepdims=True)
        drhat = dyr * kg
        mean_kr = jnp.sum(drhat * rhat, axis=-1, keepdims=True) * (1.0 / ROPE)
        dkr_ref[...] = jnp.where(m_r, rr * (drhat - rhat * mean_kr), 0.0).astype(dkr_ref.dtype)

        @pl.when(i % tps == 0)
        def _():
            st_ref[...] = jnp.zeros_like(st_ref)

        st_ref[0, 0:1, :] += dqg
        st_ref[0, 1:2, :] += dkg

    return _pcall(body, name=name, grid=(T // tm,),
                  in_specs=[_row_spec(tm, NH * LANE), _row_spec(tm, NH * LANE), _row_spec(tm, NH * NOPE),
                            _row_spec(tm, NH * LANE), _row_spec(tm, NH * LANE + NH * NOPE), kr_spec, slab, slab, vec, vec],
                  out_specs=[_row_spec(tm, NH * LANE), _row_spec(tm, NH * LANE + NH * NOPE), slab,
                             pl.BlockSpec((1, SUBLANE, LANE), lambda i: (i // tps, 0, 0))],
                  out_shape=[_sds((T, NH * LANE), MX), _sds((T, NH * LANE + NH * NOPE), MX), _sds((T, LANE), MX),
                             _sds((nb, SUBLANE, LANE), F32)],
                  )(dq, dk, dv, qp, kvp, z, cos, sin, q_gain, k_gain)


def _lower_triangle(t):
    return lax.broadcasted_iota(jnp.int32, (t, t), 1) <= lax.broadcasted_iota(jnp.int32, (t, t), 0)


def _attn_fwd(q, k, v, *, nb, name, tq=512, comm=None):
    T = q.shape[0]
    S = T // nb
    tq = _pick(S, tq)
    nq = S // tq
    tk = tq
    npair = NH // 2

    def body(q_ref, k_ref, v_ref, o_ref, lse_ref):
        qi = pl.program_id(2)
        lane = lax.broadcasted_iota(jnp.int32, (tq, LANE), 1)
        qs = [q_ref[:, hh * LANE:(hh + 1) * LANE] for hh in range(2)]

        def block(j, carry, diagonal):
            k0 = pl.multiple_of(j * tk, tk)
            vb = v_ref[pl.ds(k0, tk), :]
            new = []
            for hh in range(2):
                m, l, acc = carry[hh]
                kb = k_ref[pl.ds(k0, tk), hh * LANE:(hh + 1) * LANE]
                s = lax.dot_general(qs[hh], kb, (((1,), (1,)), ((), ())), preferred_element_type=F32) * ATTN_SCALE
                if diagonal:
                    s = jnp.where(_lower_triangle(tq), s, NEG)
                m_new = jnp.maximum(m, jnp.max(s, axis=-1, keepdims=True))
                p = jnp.exp(s - m_new)
                alpha = jnp.exp(m - m_new)
                l = alpha * l + jnp.sum(p, axis=-1, keepdims=True)
                acc = alpha * acc + jnp.dot(p.astype(MX), vb, preferred_element_type=F32)
                new.append((m_new, l, acc))
            return tuple(new)

        init = tuple((jnp.full((tq, 1), NEG, F32), jnp.zeros((tq, 1), F32), jnp.zeros((tq, LANE), F32)) for _ in range(2))
        carry = lax.fori_loop(0, qi, lambda j, c: block(j, c, False), init)
        (m0, l0, acc0), (m1, l1, acc1) = block(qi, carry, True)
        o_ref[...] = jnp.where(lane < NOPE, acc0 / l0, acc1 / l1).astype(o_ref.dtype)
        lse_ref[...] = jnp.where(lane < NOPE, m0 + jnp.log(l0), m1 + jnp.log(l1))

    return _pcall(body, name=name, grid=(nb, npair, nq),
                  in_specs=[pl.BlockSpec((tq, 2 * LANE), lambda b, p, i: (b * nq + i, p)),
                            pl.BlockSpec((S, 2 * LANE), lambda b, p, i: (b, p)),
                            pl.BlockSpec((S, LANE), lambda b, p, i: (b, p))],
                  out_specs=[pl.BlockSpec((tq, LANE), lambda b, p, i: (b * nq + i, p)),
                             pl.BlockSpec((tq, LANE), lambda b, p, i: (b * nq + i, p))],
                  out_shape=[_sds((T, NH * NOPE), MX), _sds((T, NH * NOPE), F32)], comm=comm)(q, k, v)


def _attn_bwd(q, k, v, o, lse, do, *, nb, name, tq=512, comm=None):
    T = q.shape[0]
    S = T // nb
    tq = _pick(S, tq)
    nq = S // tq
    tk = tq
    npair = NH // 2

    def body(q_ref, k_ref, v_ref, o_ref, lse_ref, do_ref, dq_ref, dk_ref, dv_ref, delta_ref):
        lane = lax.broadcasted_iota(jnp.int32, (tq, LANE), 1)
        first = lane < NOPE
        dq_ref[...] = jnp.zeros_like(dq_ref)

        def delta_step(qi, _):
            q0 = pl.multiple_of(qi * tq, tq)
            prod = do_ref[pl.ds(q0, tq), :] * o_ref[pl.ds(q0, tq), :].astype(F32)
            d0 = jnp.sum(jnp.where(first, prod, 0.0), axis=-1, keepdims=True)
            d1 = jnp.sum(jnp.where(first, 0.0, prod), axis=-1, keepdims=True)
            delta_ref[pl.ds(q0, tq), :] = jnp.where(first, d0, d1)
            return 0

        lax.fori_loop(0, nq, delta_step, 0)

        def kv_step(kj, _):
            k0 = pl.multiple_of(kj * tk, tk)
            kbs = [k_ref[pl.ds(k0, tk), hh * LANE:(hh + 1) * LANE] for hh in range(2)]
            vb = v_ref[pl.ds(k0, tk), :]

            def q_block(qi, carry, diagonal):
                dk0, dk1, dv = carry
                dks = [dk0, dk1]
                q0 = pl.multiple_of(qi * tq, tq)
                dov = do_ref[pl.ds(q0, tq), :]
                lse_v = lse_ref[pl.ds(q0, tq), :]
                delta_v = delta_ref[pl.ds(q0, tq), :]
                for hh in range(2):
                    qh = q_ref[pl.ds(q0, tq), hh * LANE:(hh + 1) * LANE]
                    dob = jnp.where(first if hh == 0 else jnp.logical_not(first), dov, 0.0).astype(MX)
                    s = lax.dot_general(qh, kbs[hh], (((1,), (1,)), ((), ())), preferred_element_type=F32) * ATTN_SCALE
                    p = jnp.exp(s - lse_v[:, hh * NOPE:hh * NOPE + 1])
                    if diagonal:
                        p = jnp.where(_lower_triangle(tq), p, 0.0)
                    dp = lax.dot_general(dob, vb, (((1,), (1,)), ((), ())), preferred_element_type=F32)
                    ds = (p * (dp - delta_v[:, hh * NOPE:hh * NOPE + 1]) * ATTN_SCALE).astype(MX)
                    dks[hh] = dks[hh] + lax.dot_general(ds, qh, (((0,), (0,)), ((), ())), preferred_element_type=F32)
                    dv = dv + lax.dot_general(p.astype(MX), dob, (((0,), (0,)), ((), ())), preferred_element_type=F32)
                    dq_ref[pl.ds(q0, tq), hh * LANE:(hh + 1) * LANE] += jnp.dot(ds, kbs[hh], preferred_element_type=F32)
                return dks[0], dks[1], dv

            zero = jnp.zeros((tk, LANE), F32)
            carry = q_block(kj, (zero, zero, zero), True)
            dk0, dk1, dv = lax.fori_loop(kj + 1, nq, lambda qi, c: q_block(qi, c, False), carry)
            dk_ref[pl.ds(k0, tk), 0:LANE] = dk0
            dk_ref[pl.ds(k0, tk), LANE:2 * LANE] = dk1
            dv_ref[pl.ds(k0, tk), :] = dv
            return 0

        lax.fori_loop(0, nq, kv_step, 0)

    pair256 = pl.BlockSpec((S, 2 * LANE), lambda b, p: (b, p))
    pair128 = pl.BlockSpec((S, LANE), lambda b, p: (b, p))
    return _pcall(body, name=name, grid=(nb, npair),
                  in_specs=[pair256, pair256, pair128, pair128, pair128, pair128],
                  out_specs=[pair256, pair256, pair128],
                  out_shape=[_sds((T, NH * LANE), F32), _sds((T, NH * LANE), F32), _sds((T, NH * NOPE), F32)],
                  scratch=[pltpu.VMEM((S, LANE), F32)], comm=comm)(q, k, v, o, lse, do)


class _NoExchange:
    def __init__(self, weights):
        self.W, self.G = weights, {}

    def rider(self, name):
        return None

    def landed(self, name, outs):
        pass

    def grad(self, key, g):
        self.G[key] = g


def _run(plan, fn, *args, name, **kw):
    rider = plan.rider(name)
    if rider is None:
        return fn(*args, name=name, **kw)
    outs, landed = fn(*args, name=name, comm=rider, **kw)
    plan.landed(name, landed)
    return outs


def _layer_fwd_bwd(x, tgt, mod3, cos, sin, plan, P):
    nb = mod3.shape[0]
    W = plan.W
    h1 = _fwd_block(x, None, mod3, P["norm_ffn1"], sub=0, coef=0.0, name="fwd_norm1")
    gu1, a1 = _run(plan, _ffn_up, h1, W["ffn1_in"], name="ffn1_up")
    f1 = _mm(a1, W["ffn1_out"], mode="nn", out_dtype=F32, name="ffn1_out", tk=DFF)
    x1, h2 = _fwd_block(x, f1, mod3, P["norm_mix"], sub=1, coef=0.5, name="fwd_norm2")
    z = _mm(h2, W["w_in"], mode="nn", out_dtype=F32, name="mix_in", tn=1664)
    pooled, mixed, scaled = _pool_fwd(z, W["pool_grp"], P["pool_scale"], nb=nb, name="pool_fwd")
    br_pool = _mm(scaled, W["pool_proj"], mode="nn", out_dtype=F32, name="pool_proj")
    qn, kvn = _lat_fwd(z, P["q_a_norm"], P["kv_a_norm"], name="lat_fwd")
    qp = _mm(qn, W["q_up"], mode="nn", out_dtype=F32, name="q_up")
    kvp = _mm(kvn, W["kv_up"], mode="nn", out_dtype=F32, name="kv_up")
    q, k, v = _prep_fwd(qp, kvp, z, cos, sin, P["q_gain"], P["k_gain"], name="prep_fwd")
    attn, lse = _run(plan, _attn_fwd, q, k, v, nb=nb, name="attn_fwd")
    br_mla = _mm(attn, W["mla_proj"], mode="nn", out_dtype=F32, name="mla_proj")
    merged = _merge_fwd(z, br_pool, br_mla, name="merge_fwd")
    mo = _mm(merged, W["w_out"], mode="nn", out_dtype=F32, name="mix_out")
    x2, h3 = _fwd_block(x1, mo, mod3, P["norm_ffn2"], sub=2, coef=1.0, name="fwd_norm3")
    gu2, a2 = _ffn_up(h3, W["ffn2_in"], name="ffn2_up")
    f2 = _mm(a2, W["ffn2_out"], mode="nn", out_dtype=F32, name="ffn2_out", tk=DFF)
    dy, df2, st_fin, loss = _final(x2, f2, tgt, mod3, name="loss_head")

    tokens = x.shape[0]
    plan.grad("ffn2_out", _mm(a2, df2, mode="tn", out_dtype=F32, name="d_ffn2_out", tm=256, tn=D, tk=tokens, n_outer=True))
    dgu2 = _ffn_dact(df2, gu2, W["ffn2_out"], name="ffn2_dact")
    dh3 = _ffn_dh(dgu2, W["ffn2_in"], name="d_ffn2_h")
    plan.grad("ffn2_in", _ffn_dw_in(h3, dgu2, name="d_ffn2_in"))
    dx2, dmo, st3 = _run(plan, _bwd_block, x2, dh3, dy, mo, mod3, P["norm_ffn2"], sub=2, coef=1.0, name="bwd_norm3")
    plan.grad("w_out", _mm(merged, dmo, mode="tn", out_dtype=F32, name="d_mix_out"))
    dmerged = _mm(dmo, W["w_out"], mode="nt", out_dtype=F32, name="d_merged")
    dbr_pool, dbr_mla, dgates = _merge_bwd(z, br_pool, br_mla, dmerged, name="merge_bwd")
    plan.grad("pool_proj", _mm(scaled, dbr_pool, mode="tn", out_dtype=F32, name="d_pool_proj"))
    dscaled = _mm(dbr_pool, W["pool_proj"], mode="nt", out_dtype=F32, name="d_pool_scaled")
    du_pool, d_pool_grp, d_pool_scale = _pool_bwd(dscaled, mixed, pooled, W["pool_grp"], P["pool_scale"], nb=nb, name="pool_bwd")
    plan.grad("mla_proj", _mm(attn, dbr_mla, mode="tn", out_dtype=F32, name="d_mla_proj"))
    dattn = _mm(dbr_mla, W["mla_proj"], mode="nt", out_dtype=F32, name="d_attn")
    dq, dk, dv = _run(plan, _attn_bwd, q, k, v, attn, lse, dattn, nb=nb, name="attn_bwd")
    dqp, dkvp, dkr, st_prep = _prep_bwd(dq, dk, dv, qp, kvp, z, cos, sin, P["q_gain"], P["k_gain"], nb=nb, name="prep_bwd")
    plan.grad("q_up", _mm(qn, dqp, mode="tn", out_dtype=F32, name="d_q_up"))
    dqn = _mm(dqp, W["q_up"], mode="nt", out_dtype=F32, name="d_qn")
    plan.grad("kv_up", _mm(kvn, dkvp, mode="tn", out_dtype=F32, name="d_kv_up"))
    dkvn = _mm(dkvp, W["kv_up"], mode="nt", out_dtype=F32, name="d_kvn", tk=1536)
    dql, dkvl, st_q, st_kv = _lat_bwd(z, dqn, dkvn, P["q_a_norm"], P["kv_a_norm"], nb=nb, name="lat_bwd")
    dz = jnp.concatenate([du_pool, dql, dkvl, dkr, dgates], axis=1)
    plan.grad("w_in", _mm(h2, dz, mode="tn", out_dtype=F32, name="d_mix_in", tn=1664))
    dh2 = _run(plan, _mm, dz, W["w_in"], mode="nt", out_dtype=F32, name="d_mix_h", tk=1664)
    dx1, df1, st2 = _bwd_block(x1, dh2, dx2, f1, mod3, P["norm_mix"], sub=1, coef=0.5, name="bwd_norm2")
    plan.grad("ffn1_out", _mm(a1, df1, mode="tn", out_dtype=F32, name="d_ffn1_out", tm=256, tn=D, tk=tokens, n_outer=True))
    dgu1 = _run(plan, _ffn_dact, df1, gu1, W["ffn1_out"], name="ffn1_dact")
    dh1 = _run(plan, _ffn_dh, dgu1, W["ffn1_in"], name="d_ffn1_h")
    plan.grad("ffn1_in", _run(plan, _ffn_dw_in, h1, dgu1, name="d_ffn1_in"))
    grad_x, st1 = _run(plan, _bwd_block, x, dh1, dx1, None, mod3, P["norm_ffn1"], sub=0, coef=0.0, name="bwd_norm1")

    dmod = jnp.stack([st1[:, 0], st1[:, 1], st2[:, 3], st2[:, 0], st2[:, 1], st3[:, 3], st3[:, 0], st3[:, 1], st_fin[:, 0]],
                     axis=1)
    small = {
        "norm_ffn1": st1[:, 2], "norm_mix": st2[:, 2], "norm_ffn2": st3[:, 2],
        "pool_grp": d_pool_grp, "pool_scale": d_pool_scale[:, 0],
        "q_a_norm": st_q[:, 0], "kv_a_norm": st_kv[:, 0],
        "q_gain": st_prep[:, 0], "k_gain": st_prep[:, 1],
    }
    return loss, grad_x, small, dmod


def _w_in_to_kernel(w):
    k = w.shape[0]
    zeros = lambda n: jnp.zeros((k, n), w.dtype)
    return jnp.concatenate([w[:, 0:1152], zeros(KR_LANE), w[:, 1152:1184], zeros(LANE - KR_LANE - ROPE), w[:, 1184:]], axis=1)


def _w_in_from_kernel(g):
    return jnp.concatenate([g[:, 0:1152], g[:, Z_KR + KR_LANE:Z_KR + KR_LANE + ROPE], g[:, Z_GP:]], axis=1)


def _q_up_to_kernel(w):
    k = w.shape[0]
    return jnp.pad(w.reshape(k, NH, NOPE + ROPE), ((0, 0), (0, 0), (0, LANE - NOPE - ROPE))).reshape(k, NH * LANE)


def _q_up_from_kernel(g):
    k = g.shape[0]
    return g.reshape(k, NH, LANE)[:, :, :NOPE + ROPE].reshape(k, NH * (NOPE + ROPE))


def _kv_up_to_kernel(w):
    k = w.shape[0]
    w3 = w.reshape(k, NH, 2 * NOPE)
    kpart = jnp.pad(w3[:, :, :NOPE], ((0, 0), (0, 0), (0, LANE - NOPE))).reshape(k, NH * LANE)
    return jnp.concatenate([kpart, w3[:, :, NOPE:].reshape(k, NH * NOPE)], axis=1)


def _kv_up_from_kernel(g):
    k = g.shape[0]
    kpart = g[:, :NH * LANE].reshape(k, NH, LANE)[:, :, :NOPE]
    vpart = g[:, NH * LANE:].reshape(k, NH, NOPE)
    return jnp.concatenate([kpart, vpart], axis=2).reshape(k, NH * 2 * NOPE)


def _gain_slab(nope, rope):
    return jnp.concatenate([nope, rope, jnp.zeros((1, LANE - NOPE - ROPE), nope.dtype)], axis=1)


def _rope_tables(positions):
    inv_freq = 10000.0 ** (-jnp.arange(0, ROPE, 2, dtype=F32) / ROPE)
    ang = positions.astype(F32)[:, None] * inv_freq
    ang = jnp.concatenate([ang, ang], axis=-1)
    t = positions.shape[0]
    cos = jnp.concatenate([jnp.ones((t, KR_LANE), F32), jnp.cos(ang), jnp.ones((t, LANE - KR_LANE - ROPE), F32)], axis=1)
    sin = jnp.concatenate([jnp.zeros((t, KR_LANE), F32), jnp.sin(ang), jnp.zeros((t, LANE - KR_LANE - ROPE), F32)], axis=1)
    return cos, sin


def _coords():
    return lax.axis_index("x"), lax.axis_index("y"), lax.axis_index("c")


HBM_SPEC = pl.BlockSpec(memory_space=pl.ANY)
VMEM_SPEC = pl.BlockSpec(memory_space=pltpu.VMEM)


def _gather8(v, *, name):
    r, c = v.shape

    def body(v_ref, out_ref, send_sems, recv_sems, local_sem):
        x, y, cc = _coords()
        me = 4 * x + 2 * y + cc
        mine = pltpu.make_async_copy(v_ref, out_ref.at[me], local_sem)
        mine.start()
        copies = []
        for kk in range(1, N_DEV):
            peer = (x ^ (kk >> 2), y ^ ((kk >> 1) & 1), cc ^ (kk & 1))
            cp = pltpu.make_async_remote_copy(src_ref=v_ref, dst_ref=out_ref.at[me], send_sem=send_sems.at[kk - 1],
                                              recv_sem=recv_sems.at[kk - 1], device_id=peer, device_id_type=MESH)
            cp.start()
            copies.append(cp)
        for kk in range(1, N_DEV):
            peer_slot = me ^ kk
            pltpu.make_async_remote_copy(src_ref=v_ref, dst_ref=out_ref.at[peer_slot], send_sem=send_sems.at[kk - 1],
                                         recv_sem=recv_sems.at[kk - 1], device_id=(x, y, cc), device_id_type=MESH).wait_recv()
        for cp in copies:
            cp.wait_send()
        mine.wait()

    return _pcall(body, name=name, out_shape=_sds((N_DEV, r, c), v.dtype), in_specs=[VMEM_SPEC], out_specs=VMEM_SPEC,
                  scratch=[pltpu.SemaphoreType.DMA((N_DEV - 1,)), pltpu.SemaphoreType.DMA((N_DEV - 1,)), pltpu.SemaphoreType.DMA])(v)


CHIP_RELS = ((1, 0), (0, 1), (1, 1))


def _comm_only(comm, *, name):
    _, outs = _pcall(lambda: None, name=name, out_shape=[], in_specs=[], out_specs=[], comm=comm)()
    return outs


def _gather_comm(shards):
    n = len(shards)

    def ici(ins, outs, sems, a, j, x, y, cc):
        half = ins[a].shape[0] // 2
        mine = pl.ds(cc * half, half)
        dx, dy = CHIP_RELS[j]
        return pltpu.make_async_remote_copy(src_ref=ins[a].at[mine], dst_ref=outs[a].at[2 * x + y, mine],
                                            send_sem=sems[0].at[a, j], recv_sem=sems[1].at[a, j],
                                            device_id=(x ^ dx, y ^ dy, cc), device_id_type=MESH)

    def d2d(ins, outs, sems, a, j, x, y, cc, half_of):
        half = ins[a].shape[0] // 2
        dx, dy = CHIP_RELS[j]
        landed = outs[a].at[2 * (x ^ dx) + (y ^ dy), pl.ds(half_of * half, half)]
        return pltpu.make_async_remote_copy(src_ref=landed, dst_ref=landed, send_sem=sems[2].at[a, j], recv_sem=sems[3].at[a, j],
                                            device_id=(x, y, 1 - cc), device_id_type=MESH)

    def own(ins, outs, sems, a, x, y):
        return pltpu.make_async_copy(ins[a], outs[a].at[2 * x + y], sems[4].at[a])

    def start(ins, outs, sems):
        x, y, cc = _coords()
        for a in range(n):
            own(ins, outs, sems, a, x, y).start()
            for j in range(3):
                ici(ins, outs, sems, a, j, x, y, cc).start()

    def finish(ins, outs, sems):
        x, y, cc = _coords()
        for a in range(n):
            for j in range(3):
                ici(ins, outs, sems, a, j, x, y, cc).wait_recv()
                d2d(ins, outs, sems, a, j, x, y, cc, cc).start()
        for a in range(n):
            for j in range(3):
                d2d(ins, outs, sems, a, j, x, y, cc, 1 - cc).wait_recv()
        for a in range(n):
            for j in range(3):
                ici(ins, outs, sems, a, j, x, y, cc).wait_send()
                d2d(ins, outs, sems, a, j, x, y, cc, cc).wait_send()
            own(ins, outs, sems, a, x, y).wait()

    dma = pltpu.SemaphoreType.DMA
    return _Comm(shards, [_sds((N_CHIP,) + s.shape, s.dtype) for s in shards],
                 [dma((n, 3)), dma((n, 3)), dma((n, 3)), dma((n, 3)), dma((n,))], start, finish)


def _swap_comm(parts):
    n = len(parts)

    def copy(ins, outs, sems, a):
        x, y, cc = _coords()
        half = ins[a].shape[1] // 2
        return pltpu.make_async_remote_copy(src_ref=ins[a].at[:, pl.ds((1 - cc) * half, half)], dst_ref=outs[a],
                                            send_sem=sems[0].at[a], recv_sem=sems[1].at[a], device_id=(x, y, 1 - cc),
                                            device_id_type=MESH)

    def start(ins, outs, sems):
        for a in range(n):
            copy(ins, outs, sems, a).start()

    def finish(ins, outs, sems):
        for a in range(n):
            copy(ins, outs, sems, a).wait()

    dma = pltpu.SemaphoreType.DMA
    return _Comm(parts, [_sds((p.shape[0], p.shape[1] // 2, p.shape[2]), p.dtype) for p in parts], [dma((n,)), dma((n,))],
                 start, finish)


def _add_half(full, other, cidx, *, name):
    nch, r, c = full.shape
    half = r // 2
    tr = _pick_rows(half)
    nbk = half // tr
    grid_spec = pltpu.PrefetchScalarGridSpec(
        num_scalar_prefetch=1, grid=(nch, nbk),
        in_specs=[pl.BlockSpec((1, tr, c), lambda j, i, cref: (j, cref[0] * nbk + i, 0)),
                  pl.BlockSpec((1, tr, c), lambda j, i, cref: (j, i, 0))],
        out_specs=pl.BlockSpec((1, tr, c), lambda j, i, cref: (j, i, 0)))

    def body(cref, a_ref, b_ref, o_ref):
        o_ref[...] = (a_ref[...] + b_ref[...]).astype(o_ref.dtype)

    return _pcall(body, name=name, out_shape=_sds((nch, half, c), MX), grid_spec=grid_spec)(cidx, full, other)


def _pick_rows(rows, target=512):
    best = None
    for t in range(16, min(rows, target) + 1, 16):
        if rows % t == 0:
            best = t
    return rows if best is None else best


def _exchange_comm(parts):
    n = len(parts)

    def send(ins, outs, sems, a, j, x, y, cc):
        dx, dy = CHIP_RELS[j]
        return pltpu.make_async_remote_copy(src_ref=ins[a].at[2 * (x ^ dx) + (y ^ dy)], dst_ref=outs[a].at[2 * x + y],
                                            send_sem=sems[0].at[a, j], recv_sem=sems[1].at[a, j],
                                            device_id=(x ^ dx, y ^ dy, cc), device_id_type=MESH)

    def landing(ins, outs, sems, a, j, x, y, cc):
        dx, dy = CHIP_RELS[j]
        peer_chip = 2 * (x ^ dx) + (y ^ dy)
        return pltpu.make_async_remote_copy(src_ref=ins[a].at[peer_chip], dst_ref=outs[a].at[peer_chip], send_sem=sems[0].at[a, j],
                                            recv_sem=sems[1].at[a, j], device_id=(x, y, cc), device_id_type=MESH)

    def own(ins, outs, sems, a, x, y):
        return pltpu.make_async_copy(ins[a].at[2 * x + y], outs[a].at[2 * x + y], sems[2].at[a])

    def start(ins, outs, sems):
        x, y, cc = _coords()
        for a in range(n):
            own(ins, outs, sems, a, x, y).start()
            for j in range(3):
                send(ins, outs, sems, a, j, x, y, cc).start()

    def finish(ins, outs, sems):
        x, y, cc = _coords()
        for a in range(n):
            for j in range(3):
                landing(ins, outs, sems, a, j, x, y, cc).wait_recv()
        for a in range(n):
            for j in range(3):
                send(ins, outs, sems, a, j, x, y, cc).wait_send()
            own(ins, outs, sems, a, x, y).wait()

    dma = pltpu.SemaphoreType.DMA
    return _Comm(parts, [_sds(p.shape, p.dtype) for p in parts], [dma((n, 3)), dma((n, 3)), dma((n,))], start, finish)


def _sum_chips(q, cidx, *, name):
    nch, h, c = q.shape
    tr = _pick_rows(h)
    nbk = h // tr
    grid_spec = pltpu.PrefetchScalarGridSpec(
        num_scalar_prefetch=1, grid=(nbk,),
        in_specs=[pl.BlockSpec((nch, tr, c), lambda i, cref: (0, i, 0))],
        out_specs=pl.BlockSpec((tr, c), lambda i, cref: (cref[0] * nbk + i, 0)))

    def body(cref, q_ref, o_ref):
        acc = q_ref[0].astype(F32) + q_ref[1].astype(F32)
        acc = acc + q_ref[2].astype(F32)
        o_ref[...] = acc + q_ref[3].astype(F32)

    return _pcall(body, name=name, out_shape=_sds((2 * h, c), F32), grid_spec=grid_spec)(cidx, q)


def _join_comm(fulls):
    n = len(fulls)

    def half_copy(outs, sems, a, which):
        x, y, cc = _coords()
        h = outs[a].shape[0] // 2
        rows = outs[a].at[pl.ds((cc if which == 0 else 1 - cc) * h, h)]
        return pltpu.make_async_remote_copy(src_ref=rows, dst_ref=rows, send_sem=sems[0].at[a], recv_sem=sems[1].at[a],
                                            device_id=(x, y, 1 - cc), device_id_type=MESH)

    def start(ins, outs, sems):
        for a in range(n):
            half_copy(outs, sems, a, 0).start()

    def finish(ins, outs, sems):
        for a in range(n):
            half_copy(outs, sems, a, 1).wait_recv()
        for a in range(n):
            half_copy(outs, sems, a, 0).wait_send()

    dma = pltpu.SemaphoreType.DMA
    return _Comm(fulls, [_sds(p.shape, p.dtype) for p in fulls], [dma((n,)), dma((n,))], start, finish,
                 aliases={a: a for a in range(n)})


def _ada_fwd(c_all, w, b, *, name):
    m, kdim = c_all.shape
    n = w.shape[1]
    tn = _pick(n, 1152)

    def body(c_ref, w_ref, b_ref, o_ref):
        cv = c_ref[...]
        act = (cv * _sigmoid(cv)).astype(MX)
        o_ref[...] = jnp.dot(act, w_ref[...].astype(MX), preferred_element_type=F32) + b_ref[...]

    return _pcall(body, name=name, out_shape=_sds((m, n), F32), grid=(n // tn,),
                  in_specs=[pl.BlockSpec((m, kdim), lambda j: (0, 0)), pl.BlockSpec((kdim, tn), lambda j: (0, j)),
                            pl.BlockSpec((1, tn), lambda j: (0, j))],
                  out_specs=pl.BlockSpec((m, tn), lambda j: (0, j)))(c_all, w, b)


def _ada_bwd(c_all, dmod_cols, *, name):
    m, kdim = c_all.shape
    n = dmod_cols.shape[1]
    tn = _pick(n, 1152)

    def body(c_ref, d_ref, o_ref):
        cv = c_ref[...]
        act = (cv * _sigmoid(cv)).astype(MX)
        o_ref[...] = lax.dot_general(act, d_ref[...].astype(MX), (((0,), (0,)), ((), ())), preferred_element_type=F32)

    return _pcall(body, name=name, out_shape=_sds((kdim, n), F32), grid=(n // tn,),
                  in_specs=[pl.BlockSpec((m, kdim), lambda j: (0, 0)), pl.BlockSpec((m, tn), lambda j: (0, j))],
                  out_specs=pl.BlockSpec((kdim, tn), lambda j: (0, j)))(c_all, dmod_cols)


def _sum_small(v, nseq, rows, *, name):
    n, r, c = v.shape
    extra = r - nseq * rows

    def body(v_ref, o_ref):
        acc = None
        for d in range(n):
            for s in range(nseq):
                t = v_ref[d, s * rows:(s + 1) * rows, :]
                acc = t if acc is None else acc + t
        o_ref[0:rows, :] = acc
        acc = v_ref[0, nseq * rows:, :]
        for d in range(1, n):
            acc = acc + v_ref[d, nseq * rows:, :]
        o_ref[rows:, :] = acc

    return _pcall(body, name=name, out_shape=_sds((rows + extra, c), F32), in_specs=[VMEM_SPEC], out_specs=VMEM_SPEC)(v)


def _adamw(w, g, m, v, *, name, comm=None):
    r, c = w.shape
    tr = _pick_rows(r, 256)
    spec = pl.BlockSpec((tr, c), lambda i: (i, 0))
    bc1 = 1.0 / (1.0 - ADAM_B1 ** ADAM_STEP)
    bc2 = 1.0 / (1.0 - ADAM_B2 ** ADAM_STEP)

    def body(w_ref, g_ref, m_ref, v_ref, d_ref, mo_ref, vo_ref):
        gv = g_ref[...]
        mn = ADAM_B1 * m_ref[...] + (1.0 - ADAM_B1) * gv
        vn = ADAM_B2 * v_ref[...] + (1.0 - ADAM_B2) * (gv * gv)
        mo_ref[...] = mn
        vo_ref[...] = vn
        d_ref[...] = -ADAM_LR * ((mn * bc1) / (jnp.sqrt(vn * bc2) + ADAM_EPS) + ADAM_WD * w_ref[...])

    out = _sds((r, c), F32)
    return _pcall(body, name=name, out_shape=[out, out, out], grid=(r // tr,), in_specs=[spec] * 4, out_specs=[spec] * 3,
                  comm=comm)(w, g, m, v)


BIG = ("w_ffn1_in", "w_ffn1_out", "w_in", "w_pool_proj", "w_q_up", "w_kv_up", "w_mla_proj", "w_out", "w_ffn2_in", "w_ffn2_out")
ROW_SHARDED = ("w_ffn1_out", "w_out", "w_ffn2_out")
KERNEL_NAME = {"w_ffn1_in": "ffn1_in", "w_ffn1_out": "ffn1_out", "w_in": "w_in", "w_pool_proj": "pool_proj", "w_q_up": "q_up",
               "w_kv_up": "kv_up", "w_mla_proj": "mla_proj", "w_out": "w_out", "w_ffn2_in": "ffn2_in", "w_ffn2_out": "ffn2_out"}
WEIGHTS = ("w_ada", "b_ada", "norm_ffn1", "w_ffn1_in", "w_ffn1_out", "norm_mix", "w_in", "pool_grp", "pool_scale", "w_pool_proj",
           "q_a_norm", "w_q_up", "kv_a_norm", "w_kv_up", "q_norm_nope", "q_norm_rope", "k_norm_nope", "k_norm_rope", "w_mla_proj",
           "w_out", "norm_ffn2", "w_ffn2_in", "w_ffn2_out")
SMALL = ("b_ada", "norm_ffn1", "norm_mix", "pool_grp", "pool_scale", "q_a_norm", "kv_a_norm", "q_norm_nope", "q_norm_rope",
         "k_norm_nope", "k_norm_rope", "norm_ffn2")
SMALL_SEQ = tuple(n for n in SMALL if n != "pool_grp")
SLAB_W = 1024


def _assemble(name, stacked):
    if name in ROW_SHARDED:
        return stacked.reshape(stacked.shape[0] * stacked.shape[1], stacked.shape[2])
    return jnp.transpose(stacked, (1, 0, 2)).reshape(stacked.shape[1], stacked.shape[0] * stacked.shape[2])


def _split(name, full):
    if name in ROW_SHARDED:
        return full.reshape(N_CHIP, full.shape[0] // N_CHIP, full.shape[1])
    return jnp.transpose(full.reshape(full.shape[0], N_CHIP, full.shape[1] // N_CHIP), (1, 0, 2))


GU = ("w_ffn1_in", "w_ffn2_in")


def _to_rows(v, width=SLAB_W):
    flat = v.reshape(-1)
    rows = -(-flat.shape[0] // width)
    return jnp.pad(flat, (0, rows * width - flat.shape[0])).reshape(rows, width)


def _pack_small(parts, names=SMALL):
    rows, spans, at = [], {}, 0
    for name in names:
        r = _to_rows(parts[name])
        spans[name] = (at, parts[name].size, parts[name].shape)
        rows.append(r)
        at += r.shape[0]
    pad = (-at) % SUBLANE
    if pad:
        rows.append(jnp.zeros((pad, SLAB_W), F32))
    return jnp.concatenate(rows, axis=0), spans


def _unpack_small(slab, spans):
    out = {}
    for name, (at, size, shape) in spans.items():
        nrow = -(-size // SLAB_W)
        out[name] = slab[at:at + nrow].reshape(-1)[:size].reshape(shape)
    return out


GATHER_ON = {"gather_ffn1": ("w_ffn1_in", "w_ffn1_out"),
             "ffn1_up": ("w_in", "w_pool_proj", "w_q_up", "w_kv_up", "w_mla_proj", "w_out"),
             "attn_fwd": ("w_ffn2_in", "w_ffn2_out")}
SWAP_ON = {"bwd_norm3": ("w_ffn2_out", "w_ffn2_in"),
           "d_mix_h": ("w_out", "w_pool_proj", "w_mla_proj", "w_q_up", "w_kv_up", "w_in"),
           "ffn1_dact": ("w_ffn1_out",),
           "bwd_norm1": ("w_ffn1_in",)}
EXCHANGE_ON = {"attn_bwd": "bwd_norm3", "d_ffn1_h": "d_mix_h", "d_ffn1_in": "ffn1_dact", "rs_exchange_last": "bwd_norm1"}


class _ExchangePlan:
    def __init__(self, shards, cidx):
        self.shards, self.cidx = shards, cidx
        self.W, self.G, self.parts, self.pre, self.reduced = {}, {}, {}, {}, {}

    def grad(self, key, g):
        self.G[key] = g

    def rider(self, name):
        if name in GATHER_ON:
            return _gather_comm([self.shards[n] for n in GATHER_ON[name]])
        if name in SWAP_ON:
            self.parts[name] = [self._stacked(n) for n in SWAP_ON[name]]
            return _swap_comm(self.parts[name])
        if name in EXCHANGE_ON:
            return _exchange_comm(self.pre[EXCHANGE_ON[name]])
        return None

    def landed(self, name, outs):
        if name in GATHER_ON:
            for n, g in zip(GATHER_ON[name], outs):
                self.W[KERNEL_NAME[n]] = self._to_kernel(n, g)
        elif name in SWAP_ON:
            self.pre[name] = [_add_half(p, o, self.cidx, name="rs_add_" + KERNEL_NAME[n])
                              for n, p, o in zip(SWAP_ON[name], self.parts[name], outs)]
        elif name in EXCHANGE_ON:
            for n, q in zip(SWAP_ON[EXCHANGE_ON[name]], outs):
                self.reduced[n] = _sum_chips(q, self.cidx, name="rs_sum_" + KERNEL_NAME[n])

    @staticmethod
    def _to_kernel(n, stacked):
        if n in GU:
            return stacked
        full = _assemble(n, stacked)
        return {"w_in": _w_in_to_kernel, "w_q_up": _q_up_to_kernel, "w_kv_up": _kv_up_to_kernel}.get(n, lambda w: w)(full)

    def _stacked(self, n):
        g = self.G[KERNEL_NAME[n]]
        if n in GU:
            return g
        full = {"w_in": _w_in_from_kernel, "w_q_up": _q_up_from_kernel, "w_kv_up": _kv_up_from_kernel}.get(n, lambda w: w)(g)
        return _split(n, full)


def kernel(x, c, positions, w_ada, b_ada, norm_ffn1, w_ffn1_in, w_ffn1_out, norm_mix, w_in, pool_grp, pool_scale, w_pool_proj, q_a_norm, w_q_up, kv_a_norm, w_kv_up, q_norm_nope, q_norm_rope, k_norm_nope, k_norm_rope, w_mla_proj, w_out, norm_ffn2, w_ffn2_in, w_ffn2_out, loss_target, m_w_ada, m_b_ada, m_norm_ffn1, m_w_ffn1_in, m_w_ffn1_out, m_norm_mix, m_w_in, m_pool_grp, m_pool_scale, m_w_pool_proj, m_q_a_norm, m_w_q_up, m_kv_a_norm, m_w_kv_up, m_q_norm_nope, m_q_norm_rope, m_k_norm_nope, m_k_norm_rope, m_w_mla_proj, m_w_out, m_norm_ffn2, m_w_ffn2_in, m_w_ffn2_out, v_w_ada, v_b_ada, v_norm_ffn1, v_w_ffn1_in, v_w_ffn1_out, v_norm_mix, v_w_in, v_pool_grp, v_pool_scale, v_w_pool_proj, v_q_a_norm, v_w_q_up, v_kv_a_norm, v_w_kv_up, v_q_norm_nope, v_q_norm_rope, v_k_norm_nope, v_k_norm_rope, v_w_mla_proj, v_w_out, v_norm_ffn2, v_w_ffn2_in, v_w_ffn2_out):
    args = dict(locals())
    wts = {n: args[n][0] for n in WEIGHTS}
    mom = {n: args["m_" + n][0] for n in WEIGHTS}
    var = {n: args["v_" + n][0] for n in WEIGHTS}
    nb, seq, dm = x.shape
    tokens = nb * seq
    xi, yi, ci = _coords()
    chip = 2 * xi + yi
    dev = 4 * xi + 2 * yi + ci

    c_all = _gather8(c, name="gather_c").reshape(N_DEV * nb, dm)
    ncol = w_ada.shape[2]
    b_cols = lax.dynamic_slice_in_dim(wts["b_ada"].reshape(1, -1), chip * ncol, ncol, axis=1)
    mod_cols = _ada_fwd(c_all, wts["w_ada"], b_cols, name="ada_fwd")
    mod_all = _gather8(mod_cols, name="gather_mod")
    mine = lax.dynamic_slice_in_dim(mod_all, dev * nb, nb, axis=1)
    mod = jnp.concatenate([mine[0], mine[2], mine[4], mine[6]], axis=1)
    mod3 = mod.reshape(nb, 9, dm)

    cidx = ci.astype(jnp.int32).reshape(1)
    plan = _ExchangePlan({n: wts[n].astype(MX) for n in BIG}, cidx)
    plan.W["pool_grp"] = wts["pool_grp"].astype(MX)
    plan.landed("gather_ffn1", _comm_only(plan.rider("gather_ffn1"), name="gather_ffn1"))
    P = {"norm_ffn1": wts["norm_ffn1"].reshape(1, dm), "norm_mix": wts["norm_mix"].reshape(1, dm),
         "norm_ffn2": wts["norm_ffn2"].reshape(1, dm), "pool_scale": wts["pool_scale"].reshape(1, POOL_W),
         "q_a_norm": wts["q_a_norm"].reshape(1, QL), "kv_a_norm": wts["kv_a_norm"].reshape(1, KVL),
         "q_gain": _gain_slab(wts["q_norm_nope"].reshape(1, NOPE), wts["q_norm_rope"].reshape(1, ROPE)),
         "k_gain": _gain_slab(wts["k_norm_nope"].reshape(1, NOPE), wts["k_norm_rope"].reshape(1, ROPE))}
    cos, sin = _rope_tables(positions.reshape(tokens))

    loss8, grad_x, small, dmod = _layer_fwd_bwd(x.reshape(tokens, dm), loss_target.reshape(tokens, dm), mod3, cos, sin, plan, P)
    loss = lax.psum(loss8[0, 0], ("x", "y", "c"))

    plan.landed("rs_exchange_last", _comm_only(plan.rider("rs_exchange_last"), name="rs_exchange_last"))
    joined = _comm_only(_join_comm([plan.reduced[n] for n in BIG]), name="rs_join_halves")
    grads = {n: g for n, g in zip(BIG, joined)}

    qg, kg = small["q_gain"], small["k_gain"]
    per_seq = {"b_ada": dmod.reshape(nb, 9 * dm), "norm_ffn1": small["norm_ffn1"], "norm_mix": small["norm_mix"],
               "pool_scale": small["pool_scale"], "q_a_norm": small["q_a_norm"],
               "kv_a_norm": small["kv_a_norm"], "q_norm_nope": qg[:, :NOPE], "q_norm_rope": qg[:, NOPE:NOPE + ROPE],
               "k_norm_nope": kg[:, :NOPE], "k_norm_rope": kg[:, KR_LANE:KR_LANE + ROPE], "norm_ffn2": small["norm_ffn2"]}
    slabs, spans = [], None
    for s in range(nb):
        slab, spans = _pack_small({n: per_seq[n][s] for n in SMALL_SEQ}, SMALL_SEQ)
        slabs.append(slab)
    rows = slabs[0].shape[0]
    slabs.append(_to_rows(small["pool_grp"]))
    gathered_small = _gather8(jnp.concatenate(slabs, axis=0), name="gather_small")
    small_sum = _sum_small(gathered_small, nb, rows, name="sum_small")
    small_grads = _unpack_small(small_sum[:rows], spans)
    small_grads["pool_grp"] = small_sum[rows:]
    for n in SMALL:
        grads[n] = small_grads[n].reshape(wts[n].shape)
    at, _, _ = spans["b_ada"]
    nrow_b = 9 * dm // SLAB_W
    dmod_all = gathered_small[:, :nb * rows].reshape(N_DEV * nb, rows, SLAB_W)[:, at:at + nrow_b].reshape(N_DEV * nb, 9 * dm)
    dmod_cols = lax.dynamic_slice_in_dim(dmod_all, chip * ncol, ncol, axis=1)
    grads["w_ada"] = _ada_bwd(c_all, dmod_cols, name="ada_bwd")

    delta, new_m, new_v = {}, {}, {}
    for n in ("w_ada",) + BIG:
        delta[n], new_m[n], new_v[n] = _adamw(wts[n], grads[n], mom[n], var[n], name="adamw_" + n)
    w_slab, sp = _pack_small({n: wts[n] for n in SMALL})
    g_slab, _ = _pack_small({n: grads[n] for n in SMALL})
    m_slab, _ = _pack_small({n: mom[n] for n in SMALL})
    v_slab, _ = _pack_small({n: var[n] for n in SMALL})
    d_s, m_s, v_s = _adamw(w_slab, g_slab, m_slab, v_slab, name="adamw_small")
    for dst, slab in ((delta, d_s), (new_m, m_s), (new_v, v_s)):
        un = _unpack_small(slab, sp)
        for n in SMALL:
            dst[n] = un[n]

    def lead(a, n):
        return a.reshape((1,) + wts[n].shape)

    return (loss, grad_x.reshape(nb, seq, dm), *[lead(grads[n], n) for n in WEIGHTS], *[lead(delta[n], n) for n in WEIGHTS],
            *[lead(new_m[n], n) for n in WEIGHTS], *[lead(new_v[n], n) for n in WEIGHTS])
```

```python
import functools
import math

import jax
import jax.numpy as jnp
from jax import lax
from jax.experimental import pallas as pl
from jax.experimental.pallas import tpu as pltpu

F32 = jnp.float32
MX = jnp.bfloat16

D = 1024
DFF = 2816
NH = 8
POOL_W = 512
POOL_G = 4
QL = 384
KVL = 256
ROPE = 32
NOPE = 64
LANE = 128
SUBLANE = 8
EPS = 1e-6
ATTN_SCALE = 1.0 / math.sqrt(96.0)
NEG = -1e30

Z_UP, Z_QL, Z_KV, Z_KR, Z_GP, Z_GM, Z_W = 0, 512, 896, 1152, 1280, 2304, 3328
KR_LANE = 64
LAT_W = 1280

ADAM_LR, ADAM_B1, ADAM_B2, ADAM_EPS, ADAM_WD, ADAM_STEP = 0.001, 0.9, 0.999, 1e-08, 0.01, 10

VMEM_LIMIT = 56 * 1024 * 1024
MESH = pl.DeviceIdType.MESH
N_DEV = 8
N_CHIP = 4


class _Comm:
    def __init__(self, ins, out_shapes, sems, start, finish, aliases=None):
        self.ins, self.out_shapes, self.sems = list(ins), list(out_shapes), list(sems)
        self.start, self.finish = start, finish
        self.aliases = aliases or {}


def _pcall(body, *, name, out_shape, grid=(), in_specs=None, out_specs=None, scratch=(), grid_spec=None, aliases=None,
           comm=None):
    params = pltpu.CompilerParams(vmem_limit_bytes=VMEM_LIMIT)
    kw = dict(name=name, compiler_params=params)
    if comm is None:
        if aliases:
            kw["input_output_aliases"] = aliases
        if grid_spec is not None:
            return pl.pallas_call(body, grid_spec=grid_spec, out_shape=out_shape, **kw)
        return pl.pallas_call(body, grid=grid, in_specs=in_specs, out_specs=out_specs, scratch_shapes=scratch,
                              out_shape=out_shape, **kw)
    single = not isinstance(out_shape, (list, tuple))
    outs = [out_shape] if single else list(out_shape)
    ospecs = [out_specs] if single else list(out_specs)
    n_in, n_out, n_ci, n_co, n_scr = len(in_specs), len(outs), len(comm.ins), len(comm.out_shapes), len(scratch)
    io = dict(aliases or {})
    io.update({n_in + i: n_out + o for i, o in comm.aliases.items()})

    def riding(*refs):
        ins, cins = refs[:n_in], refs[n_in:n_in + n_ci]
        at = n_in + n_ci
        os_, couts = refs[at:at + n_out], refs[at + n_out:at + n_out + n_co]
        at += n_out + n_co
        scr, csems = refs[at:at + n_scr], refs[at + n_scr:]
        if grid:
            first = functools.reduce(jnp.logical_and, [pl.program_id(d) == 0 for d in range(len(grid))])
            last = functools.reduce(jnp.logical_and, [pl.program_id(d) == grid[d] - 1 for d in range(len(grid))])
            pl.when(first)(lambda: comm.start(cins, couts, csems))
            body(*ins, *os_, *scr)
            pl.when(last)(lambda: comm.finish(cins, couts, csems))
        else:
            comm.start(cins, couts, csems)
            body(*ins, *os_, *scr)
            comm.finish(cins, couts, csems)

    call = pl.pallas_call(riding, grid=grid, in_specs=list(in_specs) + [HBM_SPEC] * n_ci, out_specs=ospecs + [HBM_SPEC] * n_co,
                          out_shape=outs + comm.out_shapes, scratch_shapes=list(scratch) + comm.sems,
                          input_output_aliases=io, **kw)

    def run(*args):
        res = call(*args, *comm.ins)
        main = list(res[:n_out])
        return (main[0] if single else main), list(res[n_out:])

    return run


def _pick(dim, target):
    best = None
    for t in range(LANE, min(dim, target) + 1, LANE):
        if dim % t == 0:
            best = t
    return dim if best is None else best


def _sds(shape, dtype):
    return jax.ShapeDtypeStruct(shape, dtype)


def _mm(a, b, *, mode, out_dtype, name, tm=512, tn=512, tk=1024, n_outer=False, comm=None):
    if mode == "nn":
        (M, K), (K2, N) = a.shape, b.shape
    elif mode == "nt":
        (M, K), (N, K2) = a.shape, b.shape
    else:
        (K, M), (K2, N) = a.shape, b.shape
    assert K == K2, (name, a.shape, b.shape)
    tm, tn, tk = _pick(M, tm), _pick(N, tn), _pick(K, tk)
    nk = K // tk
    if n_outer:
        ij = lambda g0, g1: (g1, g0)
        grid = (N // tn, M // tm, nk)
    else:
        ij = lambda g0, g1: (g0, g1)
        grid = (M // tm, N // tn, nk)
    if mode == "tn":
        a_spec = pl.BlockSpec((tk, tm), lambda g0, g1, k: (k, ij(g0, g1)[0]))
    else:
        a_spec = pl.BlockSpec((tm, tk), lambda g0, g1, k: (ij(g0, g1)[0], k))
    if mode == "nt":
        b_spec = pl.BlockSpec((tn, tk), lambda g0, g1, k: (ij(g0, g1)[1], k))
    else:
        b_spec = pl.BlockSpec((tk, tn), lambda g0, g1, k: (k, ij(g0, g1)[1]))
    o_spec = pl.BlockSpec((tm, tn), lambda g0, g1, k: ij(g0, g1))
    dn = {"nn": (((1,), (0,)), ((), ())), "nt": (((1,), (1,)), ((), ())), "tn": (((0,), (0,)), ((), ()))}[mode]

    def dot(a_ref, b_ref):
        return lax.dot_general(a_ref[...].astype(MX), b_ref[...].astype(MX), dn, preferred_element_type=F32)

    def body_one(a_ref, b_ref, o_ref):
        o_ref[...] = dot(a_ref, b_ref).astype(o_ref.dtype)

    def body_acc(a_ref, b_ref, o_ref, acc_ref):
        k = pl.program_id(2)
        part = dot(a_ref, b_ref)

        @pl.when(k == 0)
        def _():
            acc_ref[...] = part

        @pl.when(k > 0)
        def _():
            acc_ref[...] += part

        @pl.when(k == nk - 1)
        def _():
            o_ref[...] = acc_ref[...].astype(o_ref.dtype)

    return _pcall(body_one if nk == 1 else body_acc, name=name, out_shape=_sds((M, N), out_dtype), grid=grid,
                  in_specs=[a_spec, b_spec], out_specs=o_spec, scratch=[] if nk == 1 else [pltpu.VMEM((tm, tn), F32)],
                  comm=comm)(a, b)


def _gu_shard(q):
    return (q % 2) * 2 + q // 2


def _ffn_up(h, w_st, *, name, tm=512, comm=None):
    T, dm = h.shape
    hw = w_st.shape[2]
    tm = _pick(T, tm)

    def body(h_ref, wg_ref, wu_ref, gu_ref, a_ref):
        hv = h_ref[...]
        g = jnp.dot(hv, wg_ref[0], preferred_element_type=F32)
        u = jnp.dot(hv, wu_ref[0], preferred_element_type=F32)
        gu_ref[:, :hw] = g.astype(gu_ref.dtype)
        gu_ref[:, hw:] = u.astype(gu_ref.dtype)
        a_ref[...] = (g * _sigmoid(g) * u).astype(a_ref.dtype)

    return _pcall(body, name=name, grid=(T // tm, 2),
                  in_specs=[pl.BlockSpec((tm, dm), lambda i, j: (i, 0)), pl.BlockSpec((1, dm, hw), lambda i, j: (j, 0, 0)),
                            pl.BlockSpec((1, dm, hw), lambda i, j: (2 + j, 0, 0))],
                  out_specs=[pl.BlockSpec((tm, 2 * hw), lambda i, j: (i, j)), pl.BlockSpec((tm, hw), lambda i, j: (i, j))],
                  out_shape=[_sds((T, 4 * hw), MX), _sds((T, 2 * hw), MX)], comm=comm)(h, w_st, w_st)


def _ffn_dact(df, gu, w_out, *, name, tm=512, comm=None):
    T, dm = df.shape
    hw = gu.shape[1] // 4
    tm = _pick(T, tm)

    def body(df_ref, gu_ref, wo_ref, dgu_ref):
        da = lax.dot_general(df_ref[...], wo_ref[...], (((1,), (1,)), ((), ())), preferred_element_type=F32)
        g = gu_ref[:, :hw].astype(F32)
        u = gu_ref[:, hw:].astype(F32)
        s = _sigmoid(g)
        dgu_ref[:, :hw] = (da * u * (s * (1.0 + g * (1.0 - s)))).astype(dgu_ref.dtype)
        dgu_ref[:, hw:] = (da * (g * s)).astype(dgu_ref.dtype)

    return _pcall(body, name=name, grid=(T // tm, 2),
                  in_specs=[pl.BlockSpec((tm, dm), lambda i, j: (i, 0)), pl.BlockSpec((tm, 2 * hw), lambda i, j: (i, j)),
                            pl.BlockSpec((hw, dm), lambda i, j: (j, 0))],
                  out_specs=pl.BlockSpec((tm, 2 * hw), lambda i, j: (i, j)), out_shape=_sds(gu.shape, MX),
                  comm=comm)(df, gu, w_out)


def _ffn_dh(dgu, w_st, *, name, tm=1024, comm=None):
    T = dgu.shape[0]
    _, dm, hw = w_st.shape
    tm = _pick(T, tm)

    def body(d_ref, w_ref, o_ref, acc_ref):
        q = pl.program_id(1)
        part = lax.dot_general(d_ref[...], w_ref[0], (((1,), (1,)), ((), ())), preferred_element_type=F32)

        @pl.when(q == 0)
        def _():
            acc_ref[...] = part

        @pl.when(q > 0)
        def _():
            acc_ref[...] += part

        @pl.when(q == 3)
        def _():
            o_ref[...] = acc_ref[...]

    return _pcall(body, name=name, grid=(T // tm, 4),
                  in_specs=[pl.BlockSpec((tm, hw), lambda i, q: (i, q)), pl.BlockSpec((1, dm, hw), lambda i, q: (_gu_shard(q), 0, 0))],
                  out_specs=pl.BlockSpec((tm, dm), lambda i, q: (i, 0)), out_shape=_sds((T, dm), F32),
                  scratch=[pltpu.VMEM((tm, dm), F32)], comm=comm)(dgu, w_st)


def _ffn_dw_in(h, dgu, *, name, tm=256, comm=None):
    T, dm = h.shape
    hw = dgu.shape[1] // 4
    tm = _pick(dm, tm)

    def body(h_ref, d_ref, o_ref):
        o_ref[0] = lax.dot_general(h_ref[...], d_ref[...], (((0,), (0,)), ((), ())), preferred_element_type=F32)

    return _pcall(body, name=name, grid=(4, dm // tm),
                  in_specs=[pl.BlockSpec((T, tm), lambda q, i: (0, i)), pl.BlockSpec((T, hw), lambda q, i: (0, q))],
                  out_specs=pl.BlockSpec((1, tm, hw), lambda q, i: (_gu_shard(q), i, 0)),
                  out_shape=_sds((4, dm, hw), F32), comm=comm)(h, dgu)


def _rsq(x):
    return lax.rsqrt(jnp.mean(x * x, axis=-1, keepdims=True) + EPS)


def _sigmoid(x):
    return 1.0 / (1.0 + jnp.exp(-x))


def _row_spec(tm, w):
    return pl.BlockSpec((tm, w), lambda i: (i, 0))


def _fwd_block(x_prev, f_prev, mod3, gain, *, sub, coef, name, tm=256):
    T, dm = x_prev.shape
    tps = (T // mod3.shape[0]) // tm
    has_f = f_prev is not None
    mod_spec = pl.BlockSpec((1, 9, dm), lambda i: (i // tps, 0, 0))
    vec_spec = pl.BlockSpec((1, dm), lambda i: (0, 0))

    def body(*refs):
        if has_f:
            x_ref, f_ref, mod_ref, n_ref, xo_ref, h_ref = refs
            x = x_ref[...] + coef * mod_ref[0, 3 * sub - 1:3 * sub, :] * f_ref[...]
            xo_ref[...] = x
        else:
            x_ref, mod_ref, n_ref, h_ref = refs
            x = x_ref[...]
        xn = x * _rsq(x) * n_ref[...]
        h = xn * (1.0 + mod_ref[0, 3 * sub + 1:3 * sub + 2, :]) + mod_ref[0, 3 * sub:3 * sub + 1, :]
        h_ref[...] = h.astype(h_ref.dtype)

    row = _row_spec(tm, dm)
    if has_f:
        return _pcall(body, name=name, grid=(T // tm,), in_specs=[row, row, mod_spec, vec_spec], out_specs=[row, row],
                      out_shape=[_sds((T, dm), F32), _sds((T, dm), MX)])(x_prev, f_prev, mod3, gain)
    return _pcall(body, name=name, grid=(T // tm,), in_specs=[row, mod_spec, vec_spec], out_specs=row,
                  out_shape=_sds((T, dm), MX))(x_prev, mod3, gain)


def _final(x2, f2, tgt, mod3, *, name, tm=256):
    T, dm = x2.shape
    tps = (T // mod3.shape[0]) // tm
    mod_spec = pl.BlockSpec((1, 9, dm), lambda i: (i // tps, 0, 0))
    row = _row_spec(tm, dm)
    stat_spec = pl.BlockSpec((1, SUBLANE, dm), lambda i: (i // tps, 0, 0))
    loss_spec = pl.BlockSpec((SUBLANE, LANE), lambda i: (0, 0))

    def body(x_ref, f_ref, t_ref, mod_ref, dy_ref, df_ref, st_ref, loss_ref):
        i = pl.program_id(0)
        g = mod_ref[0, 8:9, :]
        f = f_ref[...]
        err = x_ref[...] + 0.5 * g * f - t_ref[...]
        dy = err * (1.0 / dm)
        dy_ref[...] = dy
        df_ref[...] = (0.5 * g * dy).astype(df_ref.dtype)
        dgate = jnp.sum(0.5 * dy * f, axis=0, keepdims=True)
        part = 0.5 * jnp.sum(jnp.sum(err * err, axis=0, keepdims=True), axis=1, keepdims=True) * (1.0 / dm)

        @pl.when(i % tps == 0)
        def _():
            st_ref[...] = jnp.zeros_like(st_ref)

        @pl.when(i == 0)
        def _():
            loss_ref[...] = jnp.zeros_like(loss_ref)

        st_ref[0, 0:1, :] += dgate
        loss_ref[...] += jnp.broadcast_to(part, loss_ref.shape)

    return _pcall(body, name=name, grid=(T // tm,), in_specs=[row, row, row, mod_spec],
                  out_specs=[row, row, stat_spec, loss_spec],
                  out_shape=[_sds((T, dm), F32), _sds((T, dm), MX), _sds((mod3.shape[0], SUBLANE, dm), F32),
                             _sds((SUBLANE, LANE), F32)])(x2, f2, tgt, mod3)


def _bwd_block(x_cur, dh, dx_in, f_prev, mod3, gain, *, sub, coef, name, tm=256, comm=None):
    T, dm = x_cur.shape
    nb = mod3.shape[0]
    tps = (T // nb) // tm
    has_f = f_prev is not None
    mod_spec = pl.BlockSpec((1, 9, dm), lambda i: (i // tps, 0, 0))
    vec_spec = pl.BlockSpec((1, dm), lambda i: (0, 0))
    stat_spec = pl.BlockSpec((1, SUBLANE, dm), lambda i: (i // tps, 0, 0))
    row = _row_spec(tm, dm)

    def body(*refs):
        if has_f:
            x_ref, dh_ref, dxi_ref, f_ref, mod_ref, n_ref, dx_ref, df_ref, st_ref = refs
        else:
            x_ref, dh_ref, dxi_ref, mod_ref, n_ref, dx_ref, st_ref = refs
        i = pl.program_id(0)
        x = x_ref[...]
        r = _rsq(x)
        xhat = x * r
        n = n_ref[...]
        dhv = dh_ref[...]
        d_shift = jnp.sum(dhv, axis=0, keepdims=True)
        d_scale = jnp.sum(dhv * (xhat * n), axis=0, keepdims=True)
        dxn = dhv * (1.0 + mod_ref[0, 3 * sub + 1:3 * sub + 2, :])
        d_gain = jnp.sum(dxn * xhat, axis=0, keepdims=True)
        dxhat = dxn * n
        dx = dxi_ref[...] + r * (dxhat - xhat * jnp.mean(dxhat * xhat, axis=-1, keepdims=True))
        dx_ref[...] = dx

        @pl.when(i % tps == 0)
        def _():
            st_ref[...] = jnp.zeros_like(st_ref)

        st_ref[0, 0:1, :] += d_shift
        st_ref[0, 1:2, :] += d_scale
        st_ref[0, 2:3, :] += d_gain
        if has_f:
            f = f_ref[...]
            st_ref[0, 3:4, :] += jnp.sum(coef * dx * f, axis=0, keepdims=True)
            df_ref[...] = (coef * mod_ref[0, 3 * sub - 1:3 * sub, :] * dx).astype(df_ref.dtype)

    st_shape = _sds((nb, SUBLANE, dm), F32)
    if has_f:
        return _pcall(body, name=name, grid=(T // tm,), in_specs=[row, row, row, row, mod_spec, vec_spec],
                      out_specs=[row, row, stat_spec],
                      out_shape=[_sds((T, dm), F32), _sds((T, dm), MX), st_shape], comm=comm)(x_cur, dh, dx_in, f_prev, mod3, gain)
    return _pcall(body, name=name, grid=(T // tm,), in_specs=[row, row, row, mod_spec, vec_spec],
                  out_specs=[row, stat_spec], out_shape=[_sds((T, dm), F32), st_shape], comm=comm)(x_cur, dh, dx_in, mod3, gain)


def _merge_fwd(z, br_pool, br_mla, *, name, tm=256):
    T = z.shape[0]

    def body(z_ref, bp_ref, bm_ref, o_ref):
        gp = z_ref[:, Z_GP:Z_GP + D]
        gm = z_ref[:, Z_GM:Z_GM + D]
        o_ref[...] = (_sigmoid(gp) * bp_ref[...] + _sigmoid(gm) * bm_ref[...]).astype(o_ref.dtype)

    return _pcall(body, name=name, grid=(T // tm,), in_specs=[_row_spec(tm, Z_W), _row_spec(tm, D), _row_spec(tm, D)],
                  out_specs=_row_spec(tm, D), out_shape=_sds((T, D), MX))(z, br_pool, br_mla)


def _merge_bwd(z, br_pool, br_mla, dmerged, *, name, tm=256):
    T = z.shape[0]

    def body(z_ref, bp_ref, bm_ref, dm_ref, dbp_ref, dbm_ref, dg_ref):
        dm = dm_ref[...]
        sp = _sigmoid(z_ref[:, Z_GP:Z_GP + D])
        sm = _sigmoid(z_ref[:, Z_GM:Z_GM + D])
        dbp_ref[...] = (dm * sp).astype(dbp_ref.dtype)
        dbm_ref[...] = (dm * sm).astype(dbm_ref.dtype)
        dg_ref[:, :D] = (dm * bp_ref[...] * sp * (1.0 - sp)).astype(dg_ref.dtype)
        dg_ref[:, D:] = (dm * bm_ref[...] * sm * (1.0 - sm)).astype(dg_ref.dtype)

    return _pcall(body, name=name, grid=(T // tm,),
                  in_specs=[_row_spec(tm, Z_W), _row_spec(tm, D), _row_spec(tm, D), _row_spec(tm, D)],
                  out_specs=[_row_spec(tm, D), _row_spec(tm, D), _row_spec(tm, 2 * D)],
                  out_shape=[_sds((T, D), MX), _sds((T, D), MX), _sds((T, 2 * D), MX)])(z, br_pool, br_mla, dmerged)


def _shift_down(x, k, row):
    return jnp.where(row >= k, pltpu.roll(x, k, 0), 0.0)


def _shift_up(x, k, row, n):
    return jnp.where(row < n - k, pltpu.roll(x, n - k, 0), 0.0)


def _pool_fwd(z, pool_grp, pool_scale, *, nb, name):
    T = z.shape[0]
    S = T // nb
    blk = pl.BlockSpec((S, LANE), lambda b, g: (b, g))

    def body(u_ref, w_ref, s_ref, pooled_ref, mixed_ref, scaled_ref):
        g = pl.program_id(1)
        u = u_ref[...]
        row = lax.broadcasted_iota(jnp.int32, u.shape, 0)
        s2 = u + _shift_down(u, 1, row)
        s4 = s2 + _shift_down(s2, 2, row)
        s8 = s4 + _shift_down(s4, 4, row)
        s16 = s8 + _shift_down(s8, 8, row)
        win = jnp.where(g == 0, s2, jnp.where(g == 1, s4, jnp.where(g == 2, s8, s16)))
        width = lax.shift_left(jnp.int32(2), g)
        cnt = jnp.minimum(row + 1, width).astype(F32)
        pooled = (win / cnt - u).astype(MX)
        pooled_ref[...] = pooled
        mixed = jnp.dot(pooled, w_ref[0], preferred_element_type=F32)
        mixed_ref[...] = mixed
        scaled_ref[...] = (mixed * s_ref[...]).astype(scaled_ref.dtype)

    return _pcall(body, name=name, grid=(nb, POOL_G),
                  in_specs=[blk, pl.BlockSpec((1, LANE, LANE), lambda b, g: (g, 0, 0)),
                            pl.BlockSpec((1, LANE), lambda b, g: (0, g))],
                  out_specs=[blk, blk, blk],
                  out_shape=[_sds((T, POOL_W), MX), _sds((T, POOL_W), F32), _sds((T, POOL_W), MX)])(z, pool_grp, pool_scale)


def _pool_bwd(dscaled, mixed, pooled, pool_grp, pool_scale, *, nb, name):
    T = dscaled.shape[0]
    S = T // nb
    blk = pl.BlockSpec((S, LANE), lambda g, b: (b, g))

    def body(ds_ref, mixed_ref, pooled_ref, w_ref, s_ref, du_ref, dw_ref, dsc_ref):
        g = pl.program_id(0)
        b = pl.program_id(1)
        ds = ds_ref[...]
        dsc_ref[0] = jnp.sum(ds * mixed_ref[...], axis=0, keepdims=True)
        dmixed = (ds * s_ref[...]).astype(MX)
        dw = lax.dot_general(pooled_ref[...], dmixed, (((0,), (0,)), ((), ())), preferred_element_type=F32)

        @pl.when(b == 0)
        def _():
            dw_ref[0] = dw

        @pl.when(b > 0)
        def _():
            dw_ref[0] += dw
        dpooled = lax.dot_general(dmixed, w_ref[0], (((1,), (1,)), ((), ())), preferred_element_type=F32)
        row = lax.broadcasted_iota(jnp.int32, dpooled.shape, 0)
        width = lax.shift_left(jnp.int32(2), g)
        q = dpooled / jnp.minimum(row + 1, width).astype(F32)
        r2 = q + _shift_up(q, 1, row, S)
        r4 = r2 + _shift_up(r2, 2, row, S)
        r8 = r4 + _shift_up(r4, 4, row, S)
        r16 = r8 + _shift_up(r8, 8, row, S)
        win = jnp.where(g == 0, r2, jnp.where(g == 1, r4, jnp.where(g == 2, r8, r16)))
        du_ref[...] = (win - dpooled).astype(du_ref.dtype)

    return _pcall(body, name=name, grid=(POOL_G, nb),
                  in_specs=[blk, blk, blk, pl.BlockSpec((1, LANE, LANE), lambda g, b: (g, 0, 0)),
                            pl.BlockSpec((1, LANE), lambda g, b: (0, g))],
                  out_specs=[blk, pl.BlockSpec((1, LANE, LANE), lambda g, b: (g, 0, 0)),
                             pl.BlockSpec((1, 1, LANE), lambda g, b: (b, 0, g))],
                  out_shape=[_sds((T, POOL_W), MX), _sds((POOL_G, LANE, LANE), F32), _sds((nb, 1, POOL_W), F32)],
                  )(dscaled, mixed, pooled, pool_grp, pool_scale)


def _lat_fwd(z, q_gain, kv_gain, *, name, tm=256):
    T = z.shape[0]

    def body(z_ref, qg_ref, kg_ref, qn_ref, kvn_ref):
        ql = z_ref[:, Z_QL:Z_QL + QL]
        kv = z_ref[:, Z_KV:Z_KV + KVL]
        qn_ref[...] = (ql * _rsq(ql) * qg_ref[...]).astype(qn_ref.dtype)
        kvn_ref[...] = (kv * _rsq(kv) * kg_ref[...]).astype(kvn_ref.dtype)

    return _pcall(body, name=name, grid=(T // tm,),
                  in_specs=[_row_spec(tm, LAT_W), pl.BlockSpec((1, QL), lambda i: (0, 0)), pl.BlockSpec((1, KVL), lambda i: (0, 0))],
                  out_specs=[_row_spec(tm, QL), _row_spec(tm, KVL)],
                  out_shape=[_sds((T, QL), MX), _sds((T, KVL), MX)])(z, q_gain, kv_gain)


def _lat_bwd(z, dqn, dkvn, q_gain, kv_gain, *, nb, name, tm=256):
    T = z.shape[0]
    tps = (T // nb) // tm

    def norm_bwd(x, dy, gain):
        r = _rsq(x)
        xhat = x * r
        dgain = jnp.sum(dy * xhat, axis=0, keepdims=True)
        dxhat = dy * gain
        return r * (dxhat - xhat * jnp.mean(dxhat * xhat, axis=-1, keepdims=True)), dgain

    def body(z_ref, dq_ref, dkv_ref, qg_ref, kg_ref, dql_ref, dkvl_ref, sq_ref, sk_ref):
        i = pl.program_id(0)
        dql, dqg = norm_bwd(z_ref[:, Z_QL:Z_QL + QL], dq_ref[...], qg_ref[...])
        dkvl, dkg = norm_bwd(z_ref[:, Z_KV:Z_KV + KVL], dkv_ref[...], kg_ref[...])
        dql_ref[...] = dql.astype(dql_ref.dtype)
        dkvl_ref[...] = dkvl.astype(dkvl_ref.dtype)

        @pl.when(i % tps == 0)
        def _():
            sq_ref[...] = jnp.zeros_like(sq_ref)
            sk_ref[...] = jnp.zeros_like(sk_ref)

        sq_ref[0, 0:1, :] += dqg
        sk_ref[0, 0:1, :] += dkg

    return _pcall(body, name=name, grid=(T // tm,),
                  in_specs=[_row_spec(tm, LAT_W), _row_spec(tm, QL), _row_spec(tm, KVL),
                            pl.BlockSpec((1, QL), lambda i: (0, 0)), pl.BlockSpec((1, KVL), lambda i: (0, 0))],
                  out_specs=[_row_spec(tm, QL), _row_spec(tm, KVL),
                             pl.BlockSpec((1, SUBLANE, QL), lambda i: (i // tps, 0, 0)),
                             pl.BlockSpec((1, SUBLANE, KVL), lambda i: (i // tps, 0, 0))],
                  out_shape=[_sds((T, QL), MX), _sds((T, KVL), MX), _sds((nb, SUBLANE, QL), F32), _sds((nb, SUBLANE, KVL), F32)],
                  )(z, dqn, dkvn, q_gain, kv_gain)


def _lane_masks(shape):
    lane = lax.broadcasted_iota(jnp.int32, shape, len(shape) - 1)
    m_n = lane < NOPE
    m_r = jnp.logical_and(lane >= KR_LANE, lane < KR_LANE + ROPE)
    first_half = lane < KR_LANE + ROPE // 2
    return m_n, m_r, first_half


def _rot(y, first_half):
    return jnp.where(first_half, -pltpu.roll(y, LANE - ROPE // 2, 1), pltpu.roll(y, ROPE // 2, 1))


def _rot_t(v, first_half, m_r):
    return jnp.where(m_r, jnp.where(first_half, pltpu.roll(v, LANE - ROPE // 2, 1), -pltpu.roll(v, ROPE // 2, 1)), 0.0)


def _prep_fwd(qp, kvp, z, cos, sin, q_gain, k_gain, *, name, tm=256):
    T = qp.shape[0]
    slab = pl.BlockSpec((tm, LANE), lambda i: (i, 0))
    kr_spec = pl.BlockSpec((tm, LANE), lambda i: (i, Z_KR // LANE))
    vec = pl.BlockSpec((1, LANE), lambda i: (0, 0))

    def body(qp_ref, kvp_ref, kr_ref, cos_ref, sin_ref, qg_ref, kg_ref, q_ref, k_ref, v_ref):
        m_n, m_r, first_half = _lane_masks((tm, LANE))
        c = cos_ref[...]
        s = sin_ref[...]
        qg = qg_ref[...]
        kg = kg_ref[...]
        xr = kr_ref[...]
        rr = lax.rsqrt(jnp.sum(xr * xr, axis=-1, keepdims=True) * (1.0 / ROPE) + EPS)
        yr = xr * rr * kg
        kr = jnp.where(m_r, yr * c + _rot(yr, first_half) * s, 0.0)
        for h in range(NH):
            x = qp_ref[:, h * LANE:(h + 1) * LANE]
            x2 = x * x
            rn = lax.rsqrt(jnp.sum(jnp.where(m_n, x2, 0.0), axis=-1, keepdims=True) * (1.0 / NOPE) + EPS)
            rq = lax.rsqrt(jnp.sum(jnp.where(m_r, x2, 0.0), axis=-1, keepdims=True) * (1.0 / ROPE) + EPS)
            y = x * jnp.where(m_n, rn, jnp.where(m_r, rq, 0.0)) * qg
            q_ref[:, h * LANE:(h + 1) * LANE] = (y * c + _rot(y, first_half) * s).astype(q_ref.dtype)
            xk = kvp_ref[:, h * LANE:(h + 1) * LANE]
            rk = lax.rsqrt(jnp.sum(jnp.where(m_n, xk * xk, 0.0), axis=-1, keepdims=True) * (1.0 / NOPE) + EPS)
            k_ref[:, h * LANE:(h + 1) * LANE] = (jnp.where(m_n, xk * rk * kg, 0.0) + kr).astype(k_ref.dtype)
        v_ref[...] = kvp_ref[:, NH * LANE:].astype(v_ref.dtype)

    return _pcall(body, name=name, grid=(T // tm,),
                  in_specs=[_row_spec(tm, NH * LANE), _row_spec(tm, NH * LANE + NH * NOPE), kr_spec, slab, slab, vec, vec],
                  out_specs=[_row_spec(tm, NH * LANE), _row_spec(tm, NH * LANE), _row_spec(tm, NH * NOPE)],
                  out_shape=[_sds((T, NH * LANE), MX), _sds((T, NH * LANE), MX), _sds((T, NH * NOPE), MX)],
                  )(qp, kvp, z, cos, sin, q_gain, k_gain)


def _prep_bwd(dq, dk, dv, qp, kvp, z, cos, sin, q_gain, k_gain, *, nb, name, tm=256):
    T = qp.shape[0]
    tps = (T // nb) // tm
    slab = pl.BlockSpec((tm, LANE), lambda i: (i, 0))
    kr_spec = pl.BlockSpec((tm, LANE), lambda i: (i, Z_KR // LANE))
    vec = pl.BlockSpec((1, LANE), lambda i: (0, 0))

    def body(dq_ref, dk_ref, dv_ref, qp_ref, kvp_ref, kr_ref, cos_ref, sin_ref, qg_ref, kg_ref,
             dqp_ref, dkvp_ref, dkr_ref, st_ref):
        i = pl.program_id(0)
        m_n, m_r, first_half = _lane_masks((tm, LANE))
        c = cos_ref[...]
        s = sin_ref[...]
        qg = qg_ref[...]
        kg = kg_ref[...]
        dqg = jnp.zeros((1, LANE), F32)
        dkg = jnp.zeros((1, LANE), F32)
        dkr_sum = jnp.zeros((tm, LANE), F32)
        for h in range(NH):
            x = qp_ref[:, h * LANE:(h + 1) * LANE]
            x2 = x * x
            rn = lax.rsqrt(jnp.sum(jnp.where(m_n, x2, 0.0), axis=-1, keepdims=True) * (1.0 / NOPE) + EPS)
            rq = lax.rsqrt(jnp.sum(jnp.where(m_r, x2, 0.0), axis=-1, keepdims=True) * (1.0 / ROPE) + EPS)
            rfac = jnp.where(m_n, rn, jnp.where(m_r, rq, 0.0))
            xhat = x * rfac
            do = dq_ref[:, h * LANE:(h + 1) * LANE]
            dy = do * c + _rot_t(do * s, first_half, m_r)
            dqg = dqg + jnp.sum(dy * xhat, axis=0, keepdims=True)
            dxhat = dy * qg
            t = dxhat * xhat
            mean_n = jnp.sum(jnp.where(m_n, t, 0.0), axis=-1, keepdims=True) * (1.0 / NOPE)
            mean_r = jnp.sum(jnp.where(m_r, t, 0.0), axis=-1, keepdims=True) * (1.0 / ROPE)
            dqp_ref[:, h * LANE:(h + 1) * LANE] = (
                rfac * (dxhat - xhat * jnp.where(m_n, mean_n, jnp.where(m_r, mean_r, 0.0)))).astype(dqp_ref.dtype)

            xk = kvp_ref[:, h * LANE:(h + 1) * LANE]
            rk = lax.rsqrt(jnp.sum(jnp.where(m_n, xk * xk, 0.0), axis=-1, keepdims=True) * (1.0 / NOPE) + EPS)
            khat = jnp.where(m_n, xk * rk, 0.0)
            dko = dk_ref[:, h * LANE:(h + 1) * LANE]
            dkn = jnp.where(m_n, dko, 0.0)
            dkg = dkg + jnp.sum(dkn * khat, axis=0, keepdims=True)
            dkhat = dkn * kg
            mean_k = jnp.sum(dkhat * khat, axis=-1, keepdims=True) * (1.0 / NOPE)
            dkvp_ref[:, h * LANE:(h + 1) * LANE] = jnp.where(m_n, rk * (dkhat - khat * mean_k), 0.0).astype(dkvp_ref.dtype)
            dkr_sum = dkr_sum + jnp.where(m_r, dko, 0.0)
        dkvp_ref[:, NH * LANE:] = dv_ref[...].astype(dkvp_ref.dtype)

        xr = kr_ref[...]
        rr = lax.rsqrt(jnp.sum(xr * xr, axis=-1, keepdims=True) * (1.0 / ROPE) + EPS)
        rhat = xr * rr
        dyr = dkr_sum * c + _rot_t(dkr_sum * s, first_half, m_r)
        dkg = dkg + jnp.sum(dyr * rhat, axis=0, keepdims=True)
        drhat = dyr * kg
        mean_kr = jnp.sum(drhat * rhat, axis=-1, keepdims=True) * (1.0 / ROPE)
        dkr_ref[...] = jnp.where(m_r, rr * (drhat - rhat * mean_kr), 0.0).astype(dkr_ref.dtype)

        @pl.when(i % tps == 0)
        def _():
            st_ref[...] = jnp.zeros_like(st_ref)

        st_ref[0, 0:1, :] += dqg
        st_ref[0, 1:2, :] += dkg

    return _pcall(body, name=name, grid=(T // tm,),
                  in_specs=[_row_spec(tm, NH * LANE), _row_spec(tm, NH * LANE), _row_spec(tm, NH * NOPE),
                            _row_spec(tm, NH * LANE), _row_spec(tm, NH * LANE + NH * NOPE), kr_spec, slab, slab, vec, vec],
                  out_specs=[_row_spec(tm, NH * LANE), _row_spec(tm, NH * LANE + NH * NOPE), slab,
                             pl.BlockSpec((1, SUBLANE, LANE), lambda i: (i // tps, 0, 0))],
                  out_shape=[_sds((T, NH * LANE), MX), _sds((T, NH * LANE + NH * NOPE), MX), _sds((T, LANE), MX),
                             _sds((nb, SUBLANE, LANE), F32)],
                  )(dq, dk, dv, qp, kvp, z, cos, sin, q_gain, k_gain)


def _lower_triangle(t):
    return lax.broadcasted_iota(jnp.int32, (t, t), 1) <= lax.broadcasted_iota(jnp.int32, (t, t), 0)


def _attn_fwd(q, k, v, *, nb, name, tq=512, comm=None):
    T = q.shape[0]
    S = T // nb
    tq = _pick(S, tq)
    nq = S // tq
    tk = tq
    npair = NH // 2

    def body(q_ref, k_ref, v_ref, o_ref, lse_ref):
        qi = pl.program_id(2)
        lane = lax.broadcasted_iota(jnp.int32, (tq, LANE), 1)
        qs = [q_ref[:, hh * LANE:(hh + 1) * LANE] for hh in range(2)]

        def block(j, carry, diagonal):
            k0 = pl.multiple_of(j * tk, tk)
            vb = v_ref[pl.ds(k0, tk), :]
            new = []
            for hh in range(2):
                m, l, acc = carry[hh]
                kb = k_ref[pl.ds(k0, tk), hh * LANE:(hh + 1) * LANE]
                s = lax.dot_general(qs[hh], kb, (((1,), (1,)), ((), ())), preferred_element_type=F32) * ATTN_SCALE
                if diagonal:
                    s = jnp.where(_lower_triangle(tq), s, NEG)
                m_new = jnp.maximum(m, jnp.max(s, axis=-1, keepdims=True))
                p = jnp.exp(s - m_new)
                alpha = jnp.exp(m - m_new)
                l = alpha * l + jnp.sum(p, axis=-1, keepdims=True)
                acc = alpha * acc + jnp.dot(p.astype(MX), vb, preferred_element_type=F32)
                new.append((m_new, l, acc))
            return tuple(new)

        init = tuple((jnp.full((tq, 1), NEG, F32), jnp.zeros((tq, 1), F32), jnp.zeros((tq, LANE), F32)) for _ in range(2))
        carry = lax.fori_loop(0, qi, lambda j, c: block(j, c, False), init)
        (m0, l0, acc0), (m1, l1, acc1) = block(qi, carry, True)
        o_ref[...] = jnp.where(lane < NOPE, acc0 / l0, acc1 / l1).astype(o_ref.dtype)
        lse_ref[...] = jnp.where(lane < NOPE, m0 + jnp.log(l0), m1 + jnp.log(l1))

    return _pcall(body, name=name, grid=(nb, npair, nq),
                  in_specs=[pl.BlockSpec((tq, 2 * LANE), lambda b, p, i: (b * nq + i, p)),
                            pl.BlockSpec((S, 2 * LANE), lambda b, p, i: (b, p)),
                            pl.BlockSpec((S, LANE), lambda b, p, i: (b, p))],
                  out_specs=[pl.BlockSpec((tq, LANE), lambda b, p, i: (b * nq + i, p)),
                             pl.BlockSpec((tq, LANE), lambda b, p, i: (b * nq + i, p))],
                  out_shape=[_sds((T, NH * NOPE), MX), _sds((T, NH * NOPE), F32)], comm=comm)(q, k, v)


def _attn_bwd(q, k, v, o, lse, do, *, nb, name, tq=512, comm=None):
    T = q.shape[0]
    S = T // nb
    tq = _pick(S, tq)
    nq = S // tq
    tk = tq
    npair = NH // 2

    def body(q_ref, k_ref, v_ref, o_ref, lse_ref, do_ref, dq_ref, dk_ref, dv_ref, delta_ref):
        lane = lax.broadcasted_iota(jnp.int32, (tq, LANE), 1)
        first = lane < NOPE
        dq_ref[...] = jnp.zeros_like(dq_ref)

        def delta_step(qi, _):
            q0 = pl.multiple_of(qi * tq, tq)
            prod = do_ref[pl.ds(q0, tq), :] * o_ref[pl.ds(q0, tq), :].astype(F32)
            d0 = jnp.sum(jnp.where(first, prod, 0.0), axis=-1, keepdims=True)
            d1 = jnp.sum(jnp.where(first, 0.0, prod), axis=-1, keepdims=True)
            delta_ref[pl.ds(q0, tq), :] = jnp.where(first, d0, d1)
            return 0

        lax.fori_loop(0, nq, delta_step, 0)

        def kv_step(kj, _):
            k0 = pl.multiple_of(kj * tk, tk)
            kbs = [k_ref[pl.ds(k0, tk), hh * LANE:(hh + 1) * LANE] for hh in range(2)]
            vb = v_ref[pl.ds(k0, tk), :]

            def q_block(qi, carry, diagonal):
                dk0, dk1, dv = carry
                dks = [dk0, dk1]
                q0 = pl.multiple_of(qi * tq, tq)
                dov = do_ref[pl.ds(q0, tq), :]
                lse_v = lse_ref[pl.ds(q0, tq), :]
                delta_v = delta_ref[pl.ds(q0, tq), :]
                for hh in range(2):
                    qh = q_ref[pl.ds(q0, tq), hh * LANE:(hh + 1) * LANE]
                    dob = jnp.where(first if hh == 0 else jnp.logical_not(first), dov, 0.0).astype(MX)
                    s = lax.dot_general(qh, kbs[hh], (((1,), (1,)), ((), ())), preferred_element_type=F32) * ATTN_SCALE
                    p = jnp.exp(s - lse_v[:, hh * NOPE:hh * NOPE + 1])
                    if diagonal:
                        p = jnp.where(_lower_triangle(tq), p, 0.0)
                    dp = lax.dot_general(dob, vb, (((1,), (1,)), ((), ())), preferred_element_type=F32)
                    ds = (p * (dp - delta_v[:, hh * NOPE:hh * NOPE + 1]) * ATTN_SCALE).astype(MX)
                    dks[hh] = dks[hh] + lax.dot_general(ds, qh, (((0,), (0,)), ((), ())), preferred_element_type=F32)
                    dv = dv + lax.dot_general(p.astype(MX), dob, (((0,), (0,)), ((), ())), preferred_element_type=F32)
                    dq_ref[pl.ds(q0, tq), hh * LANE:(hh + 1) * LANE] += jnp.dot(ds, kbs[hh], preferred_element_type=F32)
                return dks[0], dks[1], dv

            zero = jnp.zeros((tk, LANE), F32)
            carry = q_block(kj, (zero, zero, zero), True)
            dk0, dk1, dv = lax.fori_loop(kj + 1, nq, lambda qi, c: q_block(qi, c, False), carry)
            dk_ref[pl.ds(k0, tk), 0:LANE] = dk0
            dk_ref[pl.ds(k0, tk), LANE:2 * LANE] = dk1
            dv_ref[pl.ds(k0, tk), :] = dv
            return 0

        lax.fori_loop(0, nq, kv_step, 0)

    pair256 = pl.BlockSpec((S, 2 * LANE), lambda b, p: (b, p))
    pair128 = pl.BlockSpec((S, LANE), lambda b, p: (b, p))
    return _pcall(body, name=name, grid=(nb, npair),
                  in_specs=[pair256, pair256, pair128, pair128, pair128, pair128],
                  out_specs=[pair256, pair256, pair128],
                  out_shape=[_sds((T, NH * LANE), F32), _sds((T, NH * LANE), F32), _sds((T, NH * NOPE), F32)],
                  scratch=[pltpu.VMEM((S, LANE), F32)], comm=comm)(q, k, v, o, lse, do)


class _NoExchange:
    def __init__(self, weights):
        self.W, self.G = weights, {}

    def rider(self, name):
        return None

    def landed(self, name, outs):
        pass

    def grad(self, key, g):
        self.G[key] = g


def _run(plan, fn, *args, name, **kw):
    rider = plan.rider(name)
    if rider is None:
        return fn(*args, name=name, **kw)
    outs, landed = fn(*args, name=name, comm=rider, **kw)
    plan.landed(name, landed)
    return outs


def _layer_fwd_bwd(x, tgt, mod3, cos, sin, plan, P):
    nb = mod3.shape[0]
    W = plan.W
    h1 = _fwd_block(x, None, mod3, P["norm_ffn1"], sub=0, coef=0.0, name="fwd_norm1")
    gu1, a1 = _run(plan, _ffn_up, h1, W["ffn1_in"], name="ffn1_up")
    f1 = _mm(a1, W["ffn1_out"], mode="nn", out_dtype=F32, name="ffn1_out", tm=1024, tn=1024, tk=DFF)
    x1, h2 = _fwd_block(x, f1, mod3, P["norm_mix"], sub=1, coef=0.5, name="fwd_norm2")
    z = _mm(h2, W["w_in"], mode="nn", out_dtype=F32, name="mix_in", tn=1664)
    pooled, mixed, scaled = _pool_fwd(z, W["pool_grp"], P["pool_scale"], nb=nb, name="pool_fwd")
    br_pool = _mm(scaled, W["pool_proj"], mode="nn", out_dtype=F32, name="pool_proj")
    qn, kvn = _lat_fwd(z, P["q_a_norm"], P["kv_a_norm"], name="lat_fwd")
    qp = _mm(qn, W["q_up"], mode="nn", out_dtype=F32, name="q_up")
    kvp = _mm(kvn, W["kv_up"], mode="nn", out_dtype=F32, name="kv_up")
    q, k, v = _prep_fwd(qp, kvp, z, cos, sin, P["q_gain"], P["k_gain"], name="prep_fwd")
    attn, lse = _run(plan, _attn_fwd, q, k, v, nb=nb, name="attn_fwd")
    br_mla = _mm(attn, W["mla_proj"], mode="nn", out_dtype=F32, name="mla_proj")
    merged = _merge_fwd(z, br_pool, br_mla, name="merge_fwd")
    mo = _mm(merged, W["w_out"], mode="nn", out_dtype=F32, name="mix_out")
    x2, h3 = _fwd_block(x1, mo, mod3, P["norm_ffn2"], sub=2, coef=1.0, name="fwd_norm3")
    gu2, a2 = _ffn_up(h3, W["ffn2_in"], name="ffn2_up", tm=1024)
    f2 = _mm(a2, W["ffn2_out"], mode="nn", out_dtype=F32, name="ffn2_out", tm=1024, tn=512, tk=DFF)
    dy, df2, st_fin, loss = _final(x2, f2, tgt, mod3, name="loss_head")

    tokens = x.shape[0]
    plan.grad("ffn2_out", _mm(a2, df2, mode="tn", out_dtype=F32, name="d_ffn2_out", tm=256, tn=D, tk=tokens, n_outer=True))
    dgu2 = _ffn_dact(df2, gu2, W["ffn2_out"], name="ffn2_dact")
    dh3 = _ffn_dh(dgu2, W["ffn2_in"], name="d_ffn2_h")
    plan.grad("ffn2_in", _ffn_dw_in(h3, dgu2, name="d_ffn2_in"))
    dx2, dmo, st3 = _run(plan, _bwd_block, x2, dh3, dy, mo, mod3, P["norm_ffn2"], sub=2, coef=1.0, name="bwd_norm3")
    plan.grad("w_out", _mm(merged, dmo, mode="tn", out_dtype=F32, name="d_mix_out"))
    dmerged = _mm(dmo, W["w_out"], mode="nt", out_dtype=F32, name="d_merged")
    dbr_pool, dbr_mla, dgates = _merge_bwd(z, br_pool, br_mla, dmerged, name="merge_bwd")
    plan.grad("pool_proj", _mm(scaled, dbr_pool, mode="tn", out_dtype=F32, name="d_pool_proj"))
    dscaled = _mm(dbr_pool, W["pool_proj"], mode="nt", out_dtype=F32, name="d_pool_scaled")
    du_pool, d_pool_grp, d_pool_scale = _pool_bwd(dscaled, mixed, pooled, W["pool_grp"], P["pool_scale"], nb=nb, name="pool_bwd")
    plan.grad("mla_proj", _mm(attn, dbr_mla, mode="tn", out_dtype=F32, name="d_mla_proj"))
    dattn = _mm(dbr_mla, W["mla_proj"], mode="nt", out_dtype=F32, name="d_attn")
    dq, dk, dv = _run(plan, _attn_bwd, q, k, v, attn, lse, dattn, nb=nb, name="attn_bwd")
    dqp, dkvp, dkr, st_prep = _prep_bwd(dq, dk, dv, qp, kvp, z, cos, sin, P["q_gain"], P["k_gain"], nb=nb, name="prep_bwd")
    plan.grad("q_up", _mm(qn, dqp, mode="tn", out_dtype=F32, name="d_q_up"))
    dqn = _mm(dqp, W["q_up"], mode="nt", out_dtype=F32, name="d_qn")
    plan.grad("kv_up", _mm(kvn, dkvp, mode="tn", out_dtype=F32, name="d_kv_up"))
    dkvn = _mm(dkvp, W["kv_up"], mode="nt", out_dtype=F32, name="d_kvn", tk=1536)
    dql, dkvl, st_q, st_kv = _lat_bwd(z, dqn, dkvn, P["q_a_norm"], P["kv_a_norm"], nb=nb, name="lat_bwd")
    dz = jnp.concatenate([du_pool, dql, dkvl, dkr, dgates], axis=1)
    plan.grad("w_in", _mm(h2, dz, mode="tn", out_dtype=F32, name="d_mix_in", tn=1664))
    dh2 = _run(plan, _mm, dz, W["w_in"], mode="nt", out_dtype=F32, name="d_mix_h", tk=1664)
    dx1, df1, st2 = _bwd_block(x1, dh2, dx2, f1, mod3, P["norm_mix"], sub=1, coef=0.5, name="bwd_norm2")
    plan.grad("ffn1_out", _mm(a1, df1, mode="tn", out_dtype=F32, name="d_ffn1_out", tm=256, tn=D, tk=tokens, n_outer=True))
    dgu1 = _run(plan, _ffn_dact, df1, gu1, W["ffn1_out"], name="ffn1_dact")
    dh1 = _run(plan, _ffn_dh, dgu1, W["ffn1_in"], name="d_ffn1_h")
    plan.grad("ffn1_in", _run(plan, _ffn_dw_in, h1, dgu1, name="d_ffn1_in"))
    grad_x, st1 = _run(plan, _bwd_block, x, dh1, dx1, None, mod3, P["norm_ffn1"], sub=0, coef=0.0, name="bwd_norm1")

    dmod = jnp.stack([st1[:, 0], st1[:, 1], st2[:, 3], st2[:, 0], st2[:, 1], st3[:, 3], st3[:, 0], st3[:, 1], st_fin[:, 0]],
                     axis=1)
    small = {
        "norm_ffn1": st1[:, 2], "norm_mix": st2[:, 2], "norm_ffn2": st3[:, 2],
        "pool_grp": d_pool_grp, "pool_scale": d_pool_scale[:, 0],
        "q_a_norm": st_q[:, 0], "kv_a_norm": st_kv[:, 0],
        "q_gain": st_prep[:, 0], "k_gain": st_prep[:, 1],
    }
    return loss, grad_x, small, dmod


def _w_in_to_kernel(w):
    k = w.shape[0]
    zeros = lambda n: jnp.zeros((k, n), w.dtype)
    return jnp.concatenate([w[:, 0:1152], zeros(KR_LANE), w[:, 1152:1184], zeros(LANE - KR_LANE - ROPE), w[:, 1184:]], axis=1)


def _w_in_from_kernel(g):
    return jnp.concatenate([g[:, 0:1152], g[:, Z_KR + KR_LANE:Z_KR + KR_LANE + ROPE], g[:, Z_GP:]], axis=1)


def _q_up_to_kernel(w):
    k = w.shape[0]
    return jnp.pad(w.reshape(k, NH, NOPE + ROPE), ((0, 0), (0, 0), (0, LANE - NOPE - ROPE))).reshape(k, NH * LANE)


def _q_up_from_kernel(g):
    k = g.shape[0]
    return g.reshape(k, NH, LANE)[:, :, :NOPE + ROPE].reshape(k, NH * (NOPE + ROPE))


def _kv_up_to_kernel(w):
    k = w.shape[0]
    w3 = w.reshape(k, NH, 2 * NOPE)
    kpart = jnp.pad(w3[:, :, :NOPE], ((0, 0), (0, 0), (0, LANE - NOPE))).reshape(k, NH * LANE)
    return jnp.concatenate([kpart, w3[:, :, NOPE:].reshape(k, NH * NOPE)], axis=1)


def _kv_up_from_kernel(g):
    k = g.shape[0]
    kpart = g[:, :NH * LANE].reshape(k, NH, LANE)[:, :, :NOPE]
    vpart = g[:, NH * LANE:].reshape(k, NH, NOPE)
    return jnp.concatenate([kpart, vpart], axis=2).reshape(k, NH * 2 * NOPE)


def _gain_slab(nope, rope):
    return jnp.concatenate([nope, rope, jnp.zeros((1, LANE - NOPE - ROPE), nope.dtype)], axis=1)


def _rope_tables(positions):
    inv_freq = 10000.0 ** (-jnp.arange(0, ROPE, 2, dtype=F32) / ROPE)
    ang = positions.astype(F32)[:, None] * inv_freq
    ang = jnp.concatenate([ang, ang], axis=-1)
    t = positions.shape[0]
    cos = jnp.concatenate([jnp.ones((t, KR_LANE), F32), jnp.cos(ang), jnp.ones((t, LANE - KR_LANE - ROPE), F32)], axis=1)
    sin = jnp.concatenate([jnp.zeros((t, KR_LANE), F32), jnp.sin(ang), jnp.zeros((t, LANE - KR_LANE - ROPE), F32)], axis=1)
    return cos, sin


def _coords():
    return lax.axis_index("x"), lax.axis_index("y"), lax.axis_index("c")


HBM_SPEC = pl.BlockSpec(memory_space=pl.ANY)
VMEM_SPEC = pl.BlockSpec(memory_space=pltpu.VMEM)


def _gather8(v, *, name, comm=None):
    r, c = v.shape

    def body(v_ref, out_ref, send_sems, recv_sems, local_sem):
        x, y, cc = _coords()
        me = 4 * x + 2 * y + cc
        mine = pltpu.make_async_copy(v_ref, out_ref.at[me], local_sem)
        mine.start()
        copies = []
        for kk in range(1, N_DEV):
            peer = (x ^ (kk >> 2), y ^ ((kk >> 1) & 1), cc ^ (kk & 1))
            cp = pltpu.make_async_remote_copy(src_ref=v_ref, dst_ref=out_ref.at[me], send_sem=send_sems.at[kk - 1],
                                              recv_sem=recv_sems.at[kk - 1], device_id=peer, device_id_type=MESH)
            cp.start()
            copies.append(cp)
        for kk in range(1, N_DEV):
            peer_slot = me ^ kk
            pltpu.make_async_remote_copy(src_ref=v_ref, dst_ref=out_ref.at[peer_slot], send_sem=send_sems.at[kk - 1],
                                         recv_sem=recv_sems.at[kk - 1], device_id=(x, y, cc), device_id_type=MESH).wait_recv()
        for cp in copies:
            cp.wait_send()
        mine.wait()

    return _pcall(body, name=name, out_shape=_sds((N_DEV, r, c), v.dtype), in_specs=[VMEM_SPEC], out_specs=VMEM_SPEC,
                  scratch=[pltpu.SemaphoreType.DMA((N_DEV - 1,)), pltpu.SemaphoreType.DMA((N_DEV - 1,)), pltpu.SemaphoreType.DMA],
                  comm=comm)(v)


CHIP_RELS = ((1, 0), (0, 1), (1, 1))


def _comm_only(comm, *, name):
    _, outs = _pcall(lambda: None, name=name, out_shape=[], in_specs=[], out_specs=[], comm=comm)()
    return outs


def _gather_comm(shards):
    n = len(shards)

    def ici(ins, outs, sems, a, j, x, y, cc):
        half = ins[a].shape[0] // 2
        mine = pl.ds(cc * half, half)
        dx, dy = CHIP_RELS[j]
        return pltpu.make_async_remote_copy(src_ref=ins[a].at[mine], dst_ref=outs[a].at[2 * x + y, mine],
                                            send_sem=sems[0].at[a, j], recv_sem=sems[1].at[a, j],
                                            device_id=(x ^ dx, y ^ dy, cc), device_id_type=MESH)

    def d2d(ins, outs, sems, a, j, x, y, cc, half_of):
        half = ins[a].shape[0] // 2
        dx, dy = CHIP_RELS[j]
        landed = outs[a].at[2 * (x ^ dx) + (y ^ dy), pl.ds(half_of * half, half)]
        return pltpu.make_async_remote_copy(src_ref=landed, dst_ref=landed, send_sem=sems[2].at[a, j], recv_sem=sems[3].at[a, j],
                                            device_id=(x, y, 1 - cc), device_id_type=MESH)

    def own(ins, outs, sems, a, x, y):
        return pltpu.make_async_copy(ins[a], outs[a].at[2 * x + y], sems[4].at[a])

    def start(ins, outs, sems):
        x, y, cc = _coords()
        for a in range(n):
            own(ins, outs, sems, a, x, y).start()
            for j in range(3):
                ici(ins, outs, sems, a, j, x, y, cc).start()

    def finish(ins, outs, sems):
        x, y, cc = _coords()
        for a in range(n):
            for j in range(3):
                ici(ins, outs, sems, a, j, x, y, cc).wait_recv()
                d2d(ins, outs, sems, a, j, x, y, cc, cc).start()
        for a in range(n):
            for j in range(3):
                d2d(ins, outs, sems, a, j, x, y, cc, 1 - cc).wait_recv()
        for a in range(n):
            for j in range(3):
                ici(ins, outs, sems, a, j, x, y, cc).wait_send()
                d2d(ins, outs, sems, a, j, x, y, cc, cc).wait_send()
            own(ins, outs, sems, a, x, y).wait()

    dma = pltpu.SemaphoreType.DMA
    return _Comm(shards, [_sds((N_CHIP,) + s.shape, s.dtype) for s in shards],
                 [dma((n, 3)), dma((n, 3)), dma((n, 3)), dma((n, 3)), dma((n,))], start, finish)


def _swap_comm(parts):
    n = len(parts)

    def copy(ins, outs, sems, a):
        x, y, cc = _coords()
        half = ins[a].shape[1] // 2
        return pltpu.make_async_remote_copy(src_ref=ins[a].at[:, pl.ds((1 - cc) * half, half)], dst_ref=outs[a],
                                            send_sem=sems[0].at[a], recv_sem=sems[1].at[a], device_id=(x, y, 1 - cc),
                                            device_id_type=MESH)

    def start(ins, outs, sems):
        for a in range(n):
            copy(ins, outs, sems, a).start()

    def finish(ins, outs, sems):
        for a in range(n):
            copy(ins, outs, sems, a).wait()

    dma = pltpu.SemaphoreType.DMA
    return _Comm(parts, [_sds((p.shape[0], p.shape[1] // 2, p.shape[2]), p.dtype) for p in parts], [dma((n,)), dma((n,))],
                 start, finish)


def _add_half(full, other, cidx, *, name):
    nch, r, c = full.shape
    half = r // 2
    tr = _pick_rows(half)
    nbk = half // tr
    grid_spec = pltpu.PrefetchScalarGridSpec(
        num_scalar_prefetch=1, grid=(nch, nbk),
        in_specs=[pl.BlockSpec((1, tr, c), lambda j, i, cref: (j, cref[0] * nbk + i, 0)),
                  pl.BlockSpec((1, tr, c), lambda j, i, cref: (j, i, 0))],
        out_specs=pl.BlockSpec((1, tr, c), lambda j, i, cref: (j, i, 0)))

    def body(cref, a_ref, b_ref, o_ref):
        o_ref[...] = (a_ref[...] + b_ref[...]).astype(o_ref.dtype)

    return _pcall(body, name=name, out_shape=_sds((nch, half, c), MX), grid_spec=grid_spec)(cidx, full, other)


def _pick_rows(rows, target=512):
    best = None
    for t in range(16, min(rows, target) + 1, 16):
        if rows % t == 0:
            best = t
    return rows if best is None else best


def _exchange_comm(parts):
    n = len(parts)

    def send(ins, outs, sems, a, j, x, y, cc):
        dx, dy = CHIP_RELS[j]
        return pltpu.make_async_remote_copy(src_ref=ins[a].at[2 * (x ^ dx) + (y ^ dy)], dst_ref=outs[a].at[2 * x + y],
                                            send_sem=sems[0].at[a, j], recv_sem=sems[1].at[a, j],
                                            device_id=(x ^ dx, y ^ dy, cc), device_id_type=MESH)

    def landing(ins, outs, sems, a, j, x, y, cc):
        dx, dy = CHIP_RELS[j]
        peer_chip = 2 * (x ^ dx) + (y ^ dy)
        return pltpu.make_async_remote_copy(src_ref=ins[a].at[peer_chip], dst_ref=outs[a].at[peer_chip], send_sem=sems[0].at[a, j],
                                            recv_sem=sems[1].at[a, j], device_id=(x, y, cc), device_id_type=MESH)

    def own(ins, outs, sems, a, x, y):
        return pltpu.make_async_copy(ins[a].at[2 * x + y], outs[a].at[2 * x + y], sems[2].at[a])

    def start(ins, outs, sems):
        x, y, cc = _coords()
        for a in range(n):
            own(ins, outs, sems, a, x, y).start()
            for j in range(3):
                send(ins, outs, sems, a, j, x, y, cc).start()

    def finish(ins, outs, sems):
        x, y, cc = _coords()
        for a in range(n):
            for j in range(3):
                landing(ins, outs, sems, a, j, x, y, cc).wait_recv()
        for a in range(n):
            for j in range(3):
                send(ins, outs, sems, a, j, x, y, cc).wait_send()
            own(ins, outs, sems, a, x, y).wait()

    dma = pltpu.SemaphoreType.DMA
    return _Comm(parts, [_sds(p.shape, p.dtype) for p in parts], [dma((n, 3)), dma((n, 3)), dma((n,))], start, finish)


def _sum_chips(q, cidx, *, name):
    nch, h, c = q.shape
    tr = _pick_rows(h)
    nbk = h // tr
    grid_spec = pltpu.PrefetchScalarGridSpec(
        num_scalar_prefetch=1, grid=(nbk,),
        in_specs=[pl.BlockSpec((nch, tr, c), lambda i, cref: (0, i, 0))],
        out_specs=pl.BlockSpec((tr, c), lambda i, cref: (cref[0] * nbk + i, 0)))

    def body(cref, q_ref, o_ref):
        acc = q_ref[0].astype(F32) + q_ref[1].astype(F32)
        acc = acc + q_ref[2].astype(F32)
        o_ref[...] = acc + q_ref[3].astype(F32)

    return _pcall(body, name=name, out_shape=_sds((2 * h, c), F32), grid_spec=grid_spec)(cidx, q)


def _join_comm(fulls):
    n = len(fulls)

    def half_copy(outs, sems, a, which):
        x, y, cc = _coords()
        h = outs[a].shape[0] // 2
        rows = outs[a].at[pl.ds((cc if which == 0 else 1 - cc) * h, h)]
        return pltpu.make_async_remote_copy(src_ref=rows, dst_ref=rows, send_sem=sems[0].at[a], recv_sem=sems[1].at[a],
                                            device_id=(x, y, 1 - cc), device_id_type=MESH)

    def start(ins, outs, sems):
        for a in range(n):
            half_copy(outs, sems, a, 0).start()

    def finish(ins, outs, sems):
        for a in range(n):
            half_copy(outs, sems, a, 1).wait_recv()
        for a in range(n):
            half_copy(outs, sems, a, 0).wait_send()

    dma = pltpu.SemaphoreType.DMA
    return _Comm(fulls, [_sds(p.shape, p.dtype) for p in fulls], [dma((n,)), dma((n,))], start, finish,
                 aliases={a: a for a in range(n)})


def _ada_fwd(c_all, w, b, *, name):
    m, kdim = c_all.shape
    n = w.shape[1]
    tn = _pick(n, 1152)

    def body(c_ref, w_ref, b_ref, o_ref):
        cv = c_ref[...]
        act = (cv * _sigmoid(cv)).astype(MX)
        o_ref[...] = jnp.dot(act, w_ref[...].astype(MX), preferred_element_type=F32) + b_ref[...]

    return _pcall(body, name=name, out_shape=_sds((m, n), F32), grid=(n // tn,),
                  in_specs=[pl.BlockSpec((m, kdim), lambda j: (0, 0)), pl.BlockSpec((kdim, tn), lambda j: (0, j)),
                            pl.BlockSpec((1, tn), lambda j: (0, j))],
                  out_specs=pl.BlockSpec((m, tn), lambda j: (0, j)))(c_all, w, b)


def _ada_bwd(c_all, dmod_cols, *, name):
    m, kdim = c_all.shape
    n = dmod_cols.shape[1]
    tn = _pick(n, 1152)

    def body(c_ref, d_ref, o_ref):
        cv = c_ref[...]
        act = (cv * _sigmoid(cv)).astype(MX)
        o_ref[...] = lax.dot_general(act, d_ref[...].astype(MX), (((0,), (0,)), ((), ())), preferred_element_type=F32)

    return _pcall(body, name=name, out_shape=_sds((kdim, n), F32), grid=(n // tn,),
                  in_specs=[pl.BlockSpec((m, kdim), lambda j: (0, 0)), pl.BlockSpec((m, tn), lambda j: (0, j))],
                  out_specs=pl.BlockSpec((kdim, tn), lambda j: (0, j)))(c_all, dmod_cols)


def _sum_small(v, nseq, rows, *, name):
    n, r, c = v.shape
    extra = r - nseq * rows

    def body(v_ref, o_ref):
        acc = None
        for d in range(n):
            for s in range(nseq):
                t = v_ref[d, s * rows:(s + 1) * rows, :]
                acc = t if acc is None else acc + t
        o_ref[0:rows, :] = acc
        acc = v_ref[0, nseq * rows:, :]
        for d in range(1, n):
            acc = acc + v_ref[d, nseq * rows:, :]
        o_ref[rows:, :] = acc

    return _pcall(body, name=name, out_shape=_sds((rows + extra, c), F32), in_specs=[VMEM_SPEC], out_specs=VMEM_SPEC)(v)


def _adamw(w, g, m, v, *, name, comm=None):
    r, c = w.shape
    tr = _pick_rows(r, 256)
    spec = pl.BlockSpec((tr, c), lambda i: (i, 0))
    bc1 = 1.0 / (1.0 - ADAM_B1 ** ADAM_STEP)
    bc2 = 1.0 / (1.0 - ADAM_B2 ** ADAM_STEP)

    def body(w_ref, g_ref, m_ref, v_ref, d_ref, mo_ref, vo_ref):
        gv = g_ref[...]
        mn = ADAM_B1 * m_ref[...] + (1.0 - ADAM_B1) * gv
        vn = ADAM_B2 * v_ref[...] + (1.0 - ADAM_B2) * (gv * gv)
        mo_ref[...] = mn
        vo_ref[...] = vn
        d_ref[...] = -ADAM_LR * ((mn * bc1) / (jnp.sqrt(vn * bc2) + ADAM_EPS) + ADAM_WD * w_ref[...])

    out = _sds((r, c), F32)
    return _pcall(body, name=name, out_shape=[out, out, out], grid=(r // tr,), in_specs=[spec] * 4, out_specs=[spec] * 3,
                  comm=comm)(w, g, m, v)


BIG = ("w_ffn1_in", "w_ffn1_out", "w_in", "w_pool_proj", "w_q_up", "w_kv_up", "w_mla_proj", "w_out", "w_ffn2_in", "w_ffn2_out")
ROW_SHARDED = ("w_ffn1_out", "w_out", "w_ffn2_out")
KERNEL_NAME = {"w_ffn1_in": "ffn1_in", "w_ffn1_out": "ffn1_out", "w_in": "w_in", "w_pool_proj": "pool_proj", "w_q_up": "q_up",
               "w_kv_up": "kv_up", "w_mla_proj": "mla_proj", "w_out": "w_out", "w_ffn2_in": "ffn2_in", "w_ffn2_out": "ffn2_out"}
WEIGHTS = ("w_ada", "b_ada", "norm_ffn1", "w_ffn1_in", "w_ffn1_out", "norm_mix", "w_in", "pool_grp", "pool_scale", "w_pool_proj",
           "q_a_norm", "w_q_up", "kv_a_norm", "w_kv_up", "q_norm_nope", "q_norm_rope", "k_norm_nope", "k_norm_rope", "w_mla_proj",
           "w_out", "norm_ffn2", "w_ffn2_in", "w_ffn2_out")
SMALL = ("b_ada", "norm_ffn1", "norm_mix", "pool_grp", "pool_scale", "q_a_norm", "kv_a_norm", "q_norm_nope", "q_norm_rope",
         "k_norm_nope", "k_norm_rope", "norm_ffn2")
SMALL_SEQ = tuple(n for n in SMALL if n != "pool_grp")
SLAB_W = 1024


def _assemble(name, stacked):
    if name in ROW_SHARDED:
        return stacked.reshape(stacked.shape[0] * stacked.shape[1], stacked.shape[2])
    return jnp.transpose(stacked, (1, 0, 2)).reshape(stacked.shape[1], stacked.shape[0] * stacked.shape[2])


def _split(name, full):
    if name in ROW_SHARDED:
        return full.reshape(N_CHIP, full.shape[0] // N_CHIP, full.shape[1])
    return jnp.transpose(full.reshape(full.shape[0], N_CHIP, full.shape[1] // N_CHIP), (1, 0, 2))


GU = ("w_ffn1_in", "w_ffn2_in")


def _to_rows(v, width=SLAB_W):
    flat = v.reshape(-1)
    rows = -(-flat.shape[0] // width)
    return jnp.pad(flat, (0, rows * width - flat.shape[0])).reshape(rows, width)


def _pack_small(parts, names=SMALL):
    rows, spans, at = [], {}, 0
    for name in names:
        r = _to_rows(parts[name])
        spans[name] = (at, parts[name].size, parts[name].shape)
        rows.append(r)
        at += r.shape[0]
    pad = (-at) % SUBLANE
    if pad:
        rows.append(jnp.zeros((pad, SLAB_W), F32))
    return jnp.concatenate(rows, axis=0), spans


def _unpack_small(slab, spans):
    out = {}
    for name, (at, size, shape) in spans.items():
        nrow = -(-size // SLAB_W)
        out[name] = slab[at:at + nrow].reshape(-1)[:size].reshape(shape)
    return out


GATHER_ON = {"gather_ffn1": ("w_ffn1_in", "w_ffn1_out"),
             "ffn1_up": ("w_in", "w_pool_proj", "w_q_up", "w_kv_up", "w_mla_proj", "w_out"),
             "attn_fwd": ("w_ffn2_in", "w_ffn2_out")}
SWAP_ON = {"bwd_norm3": ("w_ffn2_out", "w_ffn2_in"),
           "d_mix_h": ("w_out", "w_pool_proj", "w_mla_proj", "w_q_up", "w_kv_up", "w_in"),
           "ffn1_dact": ("w_ffn1_out",),
           "bwd_norm1": ("w_ffn1_in",)}
EXCHANGE_ON = {"attn_bwd": "bwd_norm3", "d_ffn1_h": "d_mix_h", "d_ffn1_in": "ffn1_dact", "gather_small": "bwd_norm1"}


class _ExchangePlan:
    def __init__(self, shards, cidx):
        self.shards, self.cidx = shards, cidx
        self.W, self.G, self.parts, self.pre, self.reduced = {}, {}, {}, {}, {}

    def grad(self, key, g):
        self.G[key] = g

    def rider(self, name):
        if name in GATHER_ON:
            return _gather_comm([self.shards[n] for n in GATHER_ON[name]])
        if name in SWAP_ON:
            self.parts[name] = [self._stacked(n) for n in SWAP_ON[name]]
            return _swap_comm(self.parts[name])
        if name in EXCHANGE_ON:
            return _exchange_comm(self.pre[EXCHANGE_ON[name]])
        return None

    def landed(self, name, outs):
        if name in GATHER_ON:
            for n, g in zip(GATHER_ON[name], outs):
                self.W[KERNEL_NAME[n]] = self._to_kernel(n, g)
        elif name in SWAP_ON:
            self.pre[name] = [_add_half(p, o, self.cidx, name="rs_add_" + KERNEL_NAME[n])
                              for n, p, o in zip(SWAP_ON[name], self.parts[name], outs)]
        elif name in EXCHANGE_ON:
            for n, q in zip(SWAP_ON[EXCHANGE_ON[name]], outs):
                self.reduced[n] = _sum_chips(q, self.cidx, name="rs_sum_" + KERNEL_NAME[n])

    @staticmethod
    def _to_kernel(n, stacked):
        if n in GU:
            return stacked
        full = _assemble(n, stacked)
        return {"w_in": _w_in_to_kernel, "w_q_up": _q_up_to_kernel, "w_kv_up": _kv_up_to_kernel}.get(n, lambda w: w)(full)

    def _stacked(self, n):
        g = self.G[KERNEL_NAME[n]]
        if n in GU:
            return g
        full = {"w_in": _w_in_from_kernel, "w_q_up": _q_up_from_kernel, "w_kv_up": _kv_up_from_kernel}.get(n, lambda w: w)(g)
        return _split(n, full)


def kernel(x, c, positions, w_ada, b_ada, norm_ffn1, w_ffn1_in, w_ffn1_out, norm_mix, w_in, pool_grp, pool_scale, w_pool_proj, q_a_norm, w_q_up, kv_a_norm, w_kv_up, q_norm_nope, q_norm_rope, k_norm_nope, k_norm_rope, w_mla_proj, w_out, norm_ffn2, w_ffn2_in, w_ffn2_out, loss_target, m_w_ada, m_b_ada, m_norm_ffn1, m_w_ffn1_in, m_w_ffn1_out, m_norm_mix, m_w_in, m_pool_grp, m_pool_scale, m_w_pool_proj, m_q_a_norm, m_w_q_up, m_kv_a_norm, m_w_kv_up, m_q_norm_nope, m_q_norm_rope, m_k_norm_nope, m_k_norm_rope, m_w_mla_proj, m_w_out, m_norm_ffn2, m_w_ffn2_in, m_w_ffn2_out, v_w_ada, v_b_ada, v_norm_ffn1, v_w_ffn1_in, v_w_ffn1_out, v_norm_mix, v_w_in, v_pool_grp, v_pool_scale, v_w_pool_proj, v_q_a_norm, v_w_q_up, v_kv_a_norm, v_w_kv_up, v_q_norm_nope, v_q_norm_rope, v_k_norm_nope, v_k_norm_rope, v_w_mla_proj, v_w_out, v_norm_ffn2, v_w_ffn2_in, v_w_ffn2_out):
    args = dict(locals())
    wts = {n: args[n][0] for n in WEIGHTS}
    mom = {n: args["m_" + n][0] for n in WEIGHTS}
    var = {n: args["v_" + n][0] for n in WEIGHTS}
    nb, seq, dm = x.shape
    tokens = nb * seq
    xi, yi, ci = _coords()
    chip = 2 * xi + yi
    dev = 4 * xi + 2 * yi + ci

    c_all = _gather8(c, name="gather_c").reshape(N_DEV * nb, dm)
    ncol = w_ada.shape[2]
    b_cols = lax.dynamic_slice_in_dim(wts["b_ada"].reshape(1, -1), chip * ncol, ncol, axis=1)
    mod_cols = _ada_fwd(c_all, wts["w_ada"], b_cols, name="ada_fwd")
    mod_all = _gather8(mod_cols, name="gather_mod")
    mine = lax.dynamic_slice_in_dim(mod_all, dev * nb, nb, axis=1)
    mod = jnp.concatenate([mine[0], mine[2], mine[4], mine[6]], axis=1)
    mod3 = mod.reshape(nb, 9, dm)

    cidx = ci.astype(jnp.int32).reshape(1)
    plan = _ExchangePlan({n: wts[n].astype(MX) for n in BIG}, cidx)
    plan.W["pool_grp"] = wts["pool_grp"].astype(MX)
    plan.landed("gather_ffn1", _comm_only(plan.rider("gather_ffn1"), name="gather_ffn1"))
    P = {"norm_ffn1": wts["norm_ffn1"].reshape(1, dm), "norm_mix": wts["norm_mix"].reshape(1, dm),
         "norm_ffn2": wts["norm_ffn2"].reshape(1, dm), "pool_scale": wts["pool_scale"].reshape(1, POOL_W),
         "q_a_norm": wts["q_a_norm"].reshape(1, QL), "kv_a_norm": wts["kv_a_norm"].reshape(1, KVL),
         "q_gain": _gain_slab(wts["q_norm_nope"].reshape(1, NOPE), wts["q_norm_rope"].reshape(1, ROPE)),
         "k_gain": _gain_slab(wts["k_norm_nope"].reshape(1, NOPE), wts["k_norm_rope"].reshape(1, ROPE))}
    cos, sin = _rope_tables(positions.reshape(tokens))

    loss8, grad_x, small, dmod = _layer_fwd_bwd(x.reshape(tokens, dm), loss_target.reshape(tokens, dm), mod3, cos, sin, plan, P)
    loss = lax.psum(loss8[0, 0], ("x", "y", "c"))


    qg, kg = small["q_gain"], small["k_gain"]
    per_seq = {"b_ada": dmod.reshape(nb, 9 * dm), "norm_ffn1": small["norm_ffn1"], "norm_mix": small["norm_mix"],
               "pool_scale": small["pool_scale"], "q_a_norm": small["q_a_norm"],
               "kv_a_norm": small["kv_a_norm"], "q_norm_nope": qg[:, :NOPE], "q_norm_rope": qg[:, NOPE:NOPE + ROPE],
               "k_norm_nope": kg[:, :NOPE], "k_norm_rope": kg[:, KR_LANE:KR_LANE + ROPE], "norm_ffn2": small["norm_ffn2"]}
    slabs, spans = [], None
    for s in range(nb):
        slab, spans = _pack_small({n: per_seq[n][s] for n in SMALL_SEQ}, SMALL_SEQ)
        slabs.append(slab)
    rows = slabs[0].shape[0]
    slabs.append(_to_rows(small["pool_grp"]))
    gathered_small = _run(plan, _gather8, jnp.concatenate(slabs, axis=0), name="gather_small")
    joined = _comm_only(_join_comm([plan.reduced[n] for n in BIG]), name="rs_join_halves")
    grads = {n: g for n, g in zip(BIG, joined)}
    small_sum = _sum_small(gathered_small, nb, rows, name="sum_small")
    small_grads = _unpack_small(small_sum[:rows], spans)
    small_grads["pool_grp"] = small_sum[rows:]
    for n in SMALL:
        grads[n] = small_grads[n].reshape(wts[n].shape)
    at, _, _ = spans["b_ada"]
    nrow_b = 9 * dm // SLAB_W
    dmod_all = gathered_small[:, :nb * rows].reshape(N_DEV * nb, rows, SLAB_W)[:, at:at + nrow_b].reshape(N_DEV * nb, 9 * dm)
    dmod_cols = lax.dynamic_slice_in_dim(dmod_all, chip * ncol, ncol, axis=1)
    grads["w_ada"] = _ada_bwd(c_all, dmod_cols, name="ada_bwd")

    delta, new_m, new_v = {}, {}, {}
    for n in ("w_ada",) + BIG:
        delta[n], new_m[n], new_v[n] = _adamw(wts[n], grads[n], mom[n], var[n], name="adamw_" + n)
    w_slab, sp = _pack_small({n: wts[n] for n in SMALL})
    g_slab, _ = _pack_small({n: grads[n] for n in SMALL})
    m_slab, _ = _pack_small({n: mom[n] for n in SMALL})
    v_slab, _ = _pack_small({n: var[n] for n in SMALL})
    d_s, m_s, v_s = _adamw(w_slab, g_slab, m_slab, v_slab, name="adamw_small")
    for dst, slab in ((delta, d_s), (new_m, m_s), (new_v, v_s)):
        un = _unpack_small(slab, sp)
        for n in SMALL:
            dst[n] = un[n]

    def lead(a, n):
        return a.reshape((1,) + wts[n].shape)

    return (loss, grad_x.reshape(nb, seq, dm), *[lead(grads[n], n) for n in WEIGHTS], *[lead(delta[n], n) for n in WEIGHTS],
            *[lead(new_m[n], n) for n in WEIGHTS], *[lead(new_v[n], n) for n in WEIGHTS])
```

```python
import functools
import math

import jax
import jax.numpy as jnp
from jax import lax
from jax.experimental import pallas as pl
from jax.experimental.pallas import tpu as pltpu

F32 = jnp.float32
MX = jnp.bfloat16

D = 1024
DFF = 2816
NH = 8
POOL_W = 512
POOL_G = 4
QL = 384
KVL = 256
ROPE = 32
NOPE = 64
LANE = 128
SUBLANE = 8
EPS = 1e-6
ATTN_SCALE = 1.0 / math.sqrt(96.0)
NEG = -1e30

Z_UP, Z_QL, Z_KV, Z_KR, Z_GP, Z_GM, Z_W = 0, 512, 896, 1152, 1280, 2304, 3328
KR_LANE = 64
LAT_W = 1280

ADAM_LR, ADAM_B1, ADAM_B2, ADAM_EPS, ADAM_WD, ADAM_STEP = 0.001, 0.9, 0.999, 1e-08, 0.01, 10

VMEM_LIMIT = 48 * 1024 * 1024
MESH = pl.DeviceIdType.MESH
N_DEV = 8
N_CHIP = 4


class _Comm:
    def __init__(self, ins, out_shapes, sems, start, finish, aliases=None):
        self.ins, self.out_shapes, self.sems = list(ins), list(out_shapes), list(sems)
        self.start, self.finish = start, finish
        self.aliases = aliases or {}


def _pcall(body, *, name, out_shape, grid=(), in_specs=None, out_specs=None, scratch=(), grid_spec=None, aliases=None,
           comm=None):
    params = pltpu.CompilerParams(vmem_limit_bytes=VMEM_LIMIT)
    kw = dict(name=name, compiler_params=params)
    if comm is None:
        if aliases:
            kw["input_output_aliases"] = aliases
        if grid_spec is not None:
            return pl.pallas_call(body, grid_spec=grid_spec, out_shape=out_shape, **kw)
        return pl.pallas_call(body, grid=grid, in_specs=in_specs, out_specs=out_specs, scratch_shapes=scratch,
                              out_shape=out_shape, **kw)
    single = not isinstance(out_shape, (list, tuple))
    outs = [out_shape] if single else list(out_shape)
    ospecs = [out_specs] if single else list(out_specs)
    n_in, n_out, n_ci, n_co, n_scr = len(in_specs), len(outs), len(comm.ins), len(comm.out_shapes), len(scratch)
    io = dict(aliases or {})
    io.update({n_in + i: n_out + o for i, o in comm.aliases.items()})

    def riding(*refs):
        ins, cins = refs[:n_in], refs[n_in:n_in + n_ci]
        at = n_in + n_ci
        os_, couts = refs[at:at + n_out], refs[at + n_out:at + n_out + n_co]
        at += n_out + n_co
        scr, csems = refs[at:at + n_scr], refs[at + n_scr:]
        if grid:
            first = functools.reduce(jnp.logical_and, [pl.program_id(d) == 0 for d in range(len(grid))])
            last = functools.reduce(jnp.logical_and, [pl.program_id(d) == grid[d] - 1 for d in range(len(grid))])
            pl.when(first)(lambda: comm.start(cins, couts, csems))
            body(*ins, *os_, *scr)
            pl.when(last)(lambda: comm.finish(cins, couts, csems))
        else:
            comm.start(cins, couts, csems)
            body(*ins, *os_, *scr)
            comm.finish(cins, couts, csems)

    call = pl.pallas_call(riding, grid=grid, in_specs=list(in_specs) + [HBM_SPEC] * n_ci, out_specs=ospecs + [HBM_SPEC] * n_co,
                          out_shape=outs + comm.out_shapes, scratch_shapes=list(scratch) + comm.sems,
                          input_output_aliases=io, **kw)

    def run(*args):
        res = call(*args, *comm.ins)
        main = list(res[:n_out])
        return (main[0] if single else main), list(res[n_out:])

    return run


def _pick(dim, target):
    best = None
    for t in range(LANE, min(dim, target) + 1, LANE):
        if dim % t == 0:
            best = t
    return dim if best is None else best


def _sds(shape, dtype):
    return jax.ShapeDtypeStruct(shape, dtype)


def _mm(a, b, *, mode, out_dtype, name, tm=512, tn=512, tk=1024, n_outer=False, comm=None):
    if mode == "nn":
        (M, K), (K2, N) = a.shape, b.shape
    elif mode == "nt":
        (M, K), (N, K2) = a.shape, b.shape
    else:
        (K, M), (K2, N) = a.shape, b.shape
    assert K == K2, (name, a.shape, b.shape)
    tm, tn, tk = _pick(M, tm), _pick(N, tn), _pick(K, tk)
    nk = K // tk
    if n_outer:
        ij = lambda g0, g1: (g1, g0)
        grid = (N // tn, M // tm, nk)
    else:
        ij = lambda g0, g1: (g0, g1)
        grid = (M // tm, N // tn, nk)
    if mode == "tn":
        a_spec = pl.BlockSpec((tk, tm), lambda g0, g1, k: (k, ij(g0, g1)[0]))
    else:
        a_spec = pl.BlockSpec((tm, tk), lambda g0, g1, k: (ij(g0, g1)[0], k))
    if mode == "nt":
        b_spec = pl.BlockSpec((tn, tk), lambda g0, g1, k: (ij(g0, g1)[1], k))
    else:
        b_spec = pl.BlockSpec((tk, tn), lambda g0, g1, k: (k, ij(g0, g1)[1]))
    o_spec = pl.BlockSpec((tm, tn), lambda g0, g1, k: ij(g0, g1))
    dn = {"nn": (((1,), (0,)), ((), ())), "nt": (((1,), (1,)), ((), ())), "tn": (((0,), (0,)), ((), ()))}[mode]

    def dot(a_ref, b_ref):
        return lax.dot_general(a_ref[...].astype(MX), b_ref[...].astype(MX), dn, preferred_element_type=F32)

    def body_one(a_ref, b_ref, o_ref):
        o_ref[...] = dot(a_ref, b_ref).astype(o_ref.dtype)

    def body_acc(a_ref, b_ref, o_ref, acc_ref):
        k = pl.program_id(2)
        part = dot(a_ref, b_ref)

        @pl.when(k == 0)
        def _():
            acc_ref[...] = part

        @pl.when(k > 0)
        def _():
            acc_ref[...] += part

        @pl.when(k == nk - 1)
        def _():
            o_ref[...] = acc_ref[...].astype(o_ref.dtype)

    return _pcall(body_one if nk == 1 else body_acc, name=name, out_shape=_sds((M, N), out_dtype), grid=grid,
                  in_specs=[a_spec, b_spec], out_specs=o_spec, scratch=[] if nk == 1 else [pltpu.VMEM((tm, tn), F32)],
                  comm=comm)(a, b)


def _gu_shard(q):
    return (q % 2) * 2 + q // 2


def _ffn_up(h, w_st, *, name, tm=512, comm=None):
    T, dm = h.shape
    hw = w_st.shape[2]
    tm = _pick(T, tm)

    def body(h_ref, wg_ref, wu_ref, gu_ref, a_ref):
        hv = h_ref[...]
        g = jnp.dot(hv, wg_ref[0], preferred_element_type=F32)
        u = jnp.dot(hv, wu_ref[0], preferred_element_type=F32)
        gu_ref[:, :hw] = g.astype(gu_ref.dtype)
        gu_ref[:, hw:] = u.astype(gu_ref.dtype)
        a_ref[...] = (g * _sigmoid(g) * u).astype(a_ref.dtype)

    return _pcall(body, name=name, grid=(T // tm, 2),
                  in_specs=[pl.BlockSpec((tm, dm), lambda i, j: (i, 0)), pl.BlockSpec((1, dm, hw), lambda i, j: (j, 0, 0)),
                            pl.BlockSpec((1, dm, hw), lambda i, j: (2 + j, 0, 0))],
                  out_specs=[pl.BlockSpec((tm, 2 * hw), lambda i, j: (i, j)), pl.BlockSpec((tm, hw), lambda i, j: (i, j))],
                  out_shape=[_sds((T, 4 * hw), MX), _sds((T, 2 * hw), MX)], comm=comm)(h, w_st, w_st)


def _ffn_dact(df, gu, w_out, *, name, tm=512, comm=None):
    T, dm = df.shape
    hw = gu.shape[1] // 4
    tm = _pick(T, tm)

    def body(df_ref, gu_ref, wo_ref, dgu_ref):
        da = lax.dot_general(df_ref[...], wo_ref[...], (((1,), (1,)), ((), ())), preferred_element_type=F32)
        g = gu_ref[:, :hw].astype(F32)
        u = gu_ref[:, hw:].astype(F32)
        s = _sigmoid(g)
        dgu_ref[:, :hw] = (da * u * (s * (1.0 + g * (1.0 - s)))).astype(dgu_ref.dtype)
        dgu_ref[:, hw:] = (da * (g * s)).astype(dgu_ref.dtype)

    return _pcall(body, name=name, grid=(T // tm, 2),
                  in_specs=[pl.BlockSpec((tm, dm), lambda i, j: (i, 0)), pl.BlockSpec((tm, 2 * hw), lambda i, j: (i, j)),
                            pl.BlockSpec((hw, dm), lambda i, j: (j, 0))],
                  out_specs=pl.BlockSpec((tm, 2 * hw), lambda i, j: (i, j)), out_shape=_sds(gu.shape, MX),
                  comm=comm)(df, gu, w_out)


def _ffn_dh(dgu, w_st, *, name, tm=1024, comm=None):
    T = dgu.shape[0]
    _, dm, hw = w_st.shape
    tm = _pick(T, tm)

    def body(d_ref, w_ref, o_ref, acc_ref):
        q = pl.program_id(1)
        part = lax.dot_general(d_ref[...], w_ref[0], (((1,), (1,)), ((), ())), preferred_element_type=F32)

        @pl.when(q == 0)
        def _():
            acc_ref[...] = part

        @pl.when(q > 0)
        def _():
            acc_ref[...] += part

        @pl.when(q == 3)
        def _():
            o_ref[...] = acc_ref[...]

    return _pcall(body, name=name, grid=(T // tm, 4),
                  in_specs=[pl.BlockSpec((tm, hw), lambda i, q: (i, q)), pl.BlockSpec((1, dm, hw), lambda i, q: (_gu_shard(q), 0, 0))],
                  out_specs=pl.BlockSpec((tm, dm), lambda i, q: (i, 0)), out_shape=_sds((T, dm), F32),
                  scratch=[pltpu.VMEM((tm, dm), F32)], comm=comm)(dgu, w_st)


def _ffn_dw_in(h, dgu, *, name, tm=256, comm=None):
    T, dm = h.shape
    hw = dgu.shape[1] // 4
    tm = _pick(dm, tm)

    def body(h_ref, d_ref, o_ref):
        o_ref[0] = lax.dot_general(h_ref[...], d_ref[...], (((0,), (0,)), ((), ())), preferred_element_type=F32)

    return _pcall(body, name=name, grid=(4, dm // tm),
                  in_specs=[pl.BlockSpec((T, tm), lambda q, i: (0, i)), pl.BlockSpec((T, hw), lambda q, i: (0, q))],
                  out_specs=pl.BlockSpec((1, tm, hw), lambda q, i: (_gu_shard(q), i, 0)),
                  out_shape=_sds((4, dm, hw), F32), comm=comm)(h, dgu)


def _rsq(x):
    return lax.rsqrt(jnp.mean(x * x, axis=-1, keepdims=True) + EPS)


def _sigmoid(x):
    return 1.0 / (1.0 + jnp.exp(-x))


def _row_spec(tm, w):
    return pl.BlockSpec((tm, w), lambda i: (i, 0))


def _fwd_block(x_prev, f_prev, mod3, gain, *, sub, coef, name, tm=256):
    T, dm = x_prev.shape
    tps = (T // mod3.shape[0]) // tm
    has_f = f_prev is not None
    mod_spec = pl.BlockSpec((1, 9, dm), lambda i: (i // tps, 0, 0))
    vec_spec = pl.BlockSpec((1, dm), lambda i: (0, 0))

    def body(*refs):
        if has_f:
            x_ref, f_ref, mod_ref, n_ref, xo_ref, h_ref = refs
            x = x_ref[...] + coef * mod_ref[0, 3 * sub - 1:3 * sub, :] * f_ref[...]
            xo_ref[...] = x
        else:
            x_ref, mod_ref, n_ref, h_ref = refs
            x = x_ref[...]
        xn = x * _rsq(x) * n_ref[...]
        h = xn * (1.0 + mod_ref[0, 3 * sub + 1:3 * sub + 2, :]) + mod_ref[0, 3 * sub:3 * sub + 1, :]
        h_ref[...] = h.astype(h_ref.dtype)

    row = _row_spec(tm, dm)
    if has_f:
        return _pcall(body, name=name, grid=(T // tm,), in_specs=[row, row, mod_spec, vec_spec], out_specs=[row, row],
                      out_shape=[_sds((T, dm), F32), _sds((T, dm), MX)])(x_prev, f_prev, mod3, gain)
    return _pcall(body, name=name, grid=(T // tm,), in_specs=[row, mod_spec, vec_spec], out_specs=row,
                  out_shape=_sds((T, dm), MX))(x_prev, mod3, gain)


def _final(x2, f2, tgt, mod3, *, name, tm=256):
    T, dm = x2.shape
    tps = (T // mod3.shape[0]) // tm
    mod_spec = pl.BlockSpec((1, 9, dm), lambda i: (i // tps, 0, 0))
    row = _row_spec(tm, dm)
    stat_spec = pl.BlockSpec((1, SUBLANE, dm), lambda i: (i // tps, 0, 0))
    loss_spec = pl.BlockSpec((SUBLANE, LANE), lambda i: (0, 0))

    def body(x_ref, f_ref, t_ref, mod_ref, dy_ref, df_ref, st_ref, loss_ref):
        i = pl.program_id(0)
        g = mod_ref[0, 8:9, :]
        f = f_ref[...]
        err = x_ref[...] + 0.5 * g * f - t_ref[...]
        dy = err * (1.0 / dm)
        dy_ref[...] = dy
        df_ref[...] = (0.5 * g * dy).astype(df_ref.dtype)
        dgate = jnp.sum(0.5 * dy * f, axis=0, keepdims=True)
        part = 0.5 * jnp.sum(jnp.sum(err * err, axis=0, keepdims=True), axis=1, keepdims=True) * (1.0 / dm)

        @pl.when(i % tps == 0)
        def _():
            st_ref[...] = jnp.zeros_like(st_ref)

        @pl.when(i == 0)
        def _():
            loss_ref[...] = jnp.zeros_like(loss_ref)

        st_ref[0, 0:1, :] += dgate
        loss_ref[...] += jnp.broadcast_to(part, loss_ref.shape)

    return _pcall(body, name=name, grid=(T // tm,), in_specs=[row, row, row, mod_spec],
                  out_specs=[row, row, stat_spec, loss_spec],
                  out_shape=[_sds((T, dm), F32), _sds((T, dm), MX), _sds((mod3.shape[0], SUBLANE, dm), F32),
                             _sds((SUBLANE, LANE), F32)])(x2, f2, tgt, mod3)


def _bwd_block(x_cur, dh, dx_in, f_prev, mod3, gain, *, sub, coef, name, tm=256, comm=None):
    T, dm = x_cur.shape
    nb = mod3.shape[0]
    tps = (T // nb) // tm
    has_f = f_prev is not None
    mod_spec = pl.BlockSpec((1, 9, dm), lambda i: (i // tps, 0, 0))
    vec_spec = pl.BlockSpec((1, dm), lambda i: (0, 0))
    stat_spec = pl.BlockSpec((1, SUBLANE, dm), lambda i: (i // tps, 0, 0))
    row = _row_spec(tm, dm)

    def body(*refs):
        if has_f:
            x_ref, dh_ref, dxi_ref, f_ref, mod_ref, n_ref, dx_ref, df_ref, st_ref = refs
        else:
            x_ref, dh_ref, dxi_ref, mod_ref, n_ref, dx_ref, st_ref = refs
        i = pl.program_id(0)
        x = x_ref[...]
        r = _rsq(x)
        xhat = x * r
        n = n_ref[...]
        dhv = dh_ref[...]
        d_shift = jnp.sum(dhv, axis=0, keepdims=True)
        d_scale = jnp.sum(dhv * (xhat * n), axis=0, keepdims=True)
        dxn = dhv * (1.0 + mod_ref[0, 3 * sub + 1:3 * sub + 2, :])
        d_gain = jnp.sum(dxn * xhat, axis=0, keepdims=True)
        dxhat = dxn * n
        dx = dxi_ref[...] + r * (dxhat - xhat * jnp.mean(dxhat * xhat, axis=-1, keepdims=True))
        dx_ref[...] = dx

        @pl.when(i % tps == 0)
        def _():
            st_ref[...] = jnp.zeros_like(st_ref)

        st_ref[0, 0:1, :] += d_shift
        st_ref[0, 1:2, :] += d_scale
        st_ref[0, 2:3, :] += d_gain
        if has_f:
            f = f_ref[...]
            st_ref[0, 3:4, :] += jnp.sum(coef * dx * f, axis=0, keepdims=True)
            df_ref[...] = (coef * mod_ref[0, 3 * sub - 1:3 * sub, :] * dx).astype(df_ref.dtype)

    st_shape = _sds((nb, SUBLANE, dm), F32)
    if has_f:
        return _pcall(body, name=name, grid=(T // tm,), in_specs=[row, row, row, row, mod_spec, vec_spec],
                      out_specs=[row, row, stat_spec],
                      out_shape=[_sds((T, dm), F32), _sds((T, dm), MX), st_shape], comm=comm)(x_cur, dh, dx_in, f_prev, mod3, gain)
    return _pcall(body, name=name, grid=(T // tm,), in_specs=[row, row, row, mod_spec, vec_spec],
                  out_specs=[row, stat_spec], out_shape=[_sds((T, dm), F32), st_shape], comm=comm)(x_cur, dh, dx_in, mod3, gain)


def _merge_fwd(z, br_pool, br_mla, *, name, tm=256):
    T = z.shape[0]

    def body(z_ref, bp_ref, bm_ref, o_ref):
        gp = z_ref[:, Z_GP:Z_GP + D]
        gm = z_ref[:, Z_GM:Z_GM + D]
        o_ref[...] = (_sigmoid(gp) * bp_ref[...] + _sigmoid(gm) * bm_ref[...]).astype(o_ref.dtype)

    return _pcall(body, name=name, grid=(T // tm,), in_specs=[_row_spec(tm, Z_W), _row_spec(tm, D), _row_spec(tm, D)],
                  out_specs=_row_spec(tm, D), out_shape=_sds((T, D), MX))(z, br_pool, br_mla)


def _merge_bwd(z, br_pool, br_mla, dmerged, *, name, tm=256):
    T = z.shape[0]

    def body(z_ref, bp_ref, bm_ref, dm_ref, dbp_ref, dbm_ref, dg_ref):
        dm = dm_ref[...]
        sp = _sigmoid(z_ref[:, Z_GP:Z_GP + D])
        sm = _sigmoid(z_ref[:, Z_GM:Z_GM + D])
        dbp_ref[...] = (dm * sp).astype(dbp_ref.dtype)
        dbm_ref[...] = (dm * sm).astype(dbm_ref.dtype)
        dg_ref[:, :D] = (dm * bp_ref[...] * sp * (1.0 - sp)).astype(dg_ref.dtype)
        dg_ref[:, D:] = (dm * bm_ref[...] * sm * (1.0 - sm)).astype(dg_ref.dtype)

    return _pcall(body, name=name, grid=(T // tm,),
                  in_specs=[_row_spec(tm, Z_W), _row_spec(tm, D), _row_spec(tm, D), _row_spec(tm, D)],
                  out_specs=[_row_spec(tm, D), _row_spec(tm, D), _row_spec(tm, 2 * D)],
                  out_shape=[_sds((T, D), MX), _sds((T, D), MX), _sds((T, 2 * D), MX)])(z, br_pool, br_mla, dmerged)


def _shift_down(x, k, row):
    return jnp.where(row >= k, pltpu.roll(x, k, 0), 0.0)


def _shift_up(x, k, row, n):
    return jnp.where(row < n - k, pltpu.roll(x, n - k, 0), 0.0)


def _pool_fwd(z, pool_grp, pool_scale, *, nb, name):
    T = z.shape[0]
    S = T // nb
    blk = pl.BlockSpec((S, LANE), lambda b, g: (b, g))

    def body(u_ref, w_ref, s_ref, pooled_ref, mixed_ref, scaled_ref):
        g = pl.program_id(1)
        u = u_ref[...]
        row = lax.broadcasted_iota(jnp.int32, u.shape, 0)
        s2 = u + _shift_down(u, 1, row)
        s4 = s2 + _shift_down(s2, 2, row)
        s8 = s4 + _shift_down(s4, 4, row)
        s16 = s8 + _shift_down(s8, 8, row)
        win = jnp.where(g == 0, s2, jnp.where(g == 1, s4, jnp.where(g == 2, s8, s16)))
        width = lax.shift_left(jnp.int32(2), g)
        cnt = jnp.minimum(row + 1, width).astype(F32)
        pooled = (win / cnt - u).astype(MX)
        pooled_ref[...] = pooled
        mixed = jnp.dot(pooled, w_ref[0], preferred_element_type=F32)
        mixed_ref[...] = mixed
        scaled_ref[...] = (mixed * s_ref[...]).astype(scaled_ref.dtype)

    return _pcall(body, name=name, grid=(nb, POOL_G),
                  in_specs=[blk, pl.BlockSpec((1, LANE, LANE), lambda b, g: (g, 0, 0)),
                            pl.BlockSpec((1, LANE), lambda b, g: (0, g))],
                  out_specs=[blk, blk, blk],
                  out_shape=[_sds((T, POOL_W), MX), _sds((T, POOL_W), F32), _sds((T, POOL_W), MX)])(z, pool_grp, pool_scale)


def _pool_bwd(dscaled, mixed, pooled, pool_grp, pool_scale, *, nb, name):
    T = dscaled.shape[0]
    S = T // nb
    blk = pl.BlockSpec((S, LANE), lambda g, b: (b, g))

    def body(ds_ref, mixed_ref, pooled_ref, w_ref, s_ref, du_ref, dw_ref, dsc_ref):
        g = pl.program_id(0)
        b = pl.program_id(1)
        ds = ds_ref[...]
        dsc_ref[0] = jnp.sum(ds * mixed_ref[...], axis=0, keepdims=True)
        dmixed = (ds * s_ref[...]).astype(MX)
        dw = lax.dot_general(pooled_ref[...], dmixed, (((0,), (0,)), ((), ())), preferred_element_type=F32)

        @pl.when(b == 0)
        def _():
            dw_ref[0] = dw

        @pl.when(b > 0)
        def _():
            dw_ref[0] += dw
        dpooled = lax.dot_general(dmixed, w_ref[0], (((1,), (1,)), ((), ())), preferred_element_type=F32)
        row = lax.broadcasted_iota(jnp.int32, dpooled.shape, 0)
        width = lax.shift_left(jnp.int32(2), g)
        q = dpooled / jnp.minimum(row + 1, width).astype(F32)
        r2 = q + _shift_up(q, 1, row, S)
        r4 = r2 + _shift_up(r2, 2, row, S)
        r8 = r4 + _shift_up(r4, 4, row, S)
        r16 = r8 + _shift_up(r8, 8, row, S)
        win = jnp.where(g == 0, r2, jnp.where(g == 1, r4, jnp.where(g == 2, r8, r16)))
        du_ref[...] = (win - dpooled).astype(du_ref.dtype)

    return _pcall(body, name=name, grid=(POOL_G, nb),
                  in_specs=[blk, blk, blk, pl.BlockSpec((1, LANE, LANE), lambda g, b: (g, 0, 0)),
                            pl.BlockSpec((1, LANE), lambda g, b: (0, g))],
                  out_specs=[blk, pl.BlockSpec((1, LANE, LANE), lambda g, b: (g, 0, 0)),
                             pl.BlockSpec((1, 1, LANE), lambda g, b: (b, 0, g))],
                  out_shape=[_sds((T, POOL_W), MX), _sds((POOL_G, LANE, LANE), F32), _sds((nb, 1, POOL_W), F32)],
                  )(dscaled, mixed, pooled, pool_grp, pool_scale)


def _lat_fwd(z, q_gain, kv_gain, *, name, tm=256):
    T = z.shape[0]

    def body(z_ref, qg_ref, kg_ref, qn_ref, kvn_ref):
        ql = z_ref[:, Z_QL:Z_QL + QL]
        kv = z_ref[:, Z_KV:Z_KV + KVL]
        qn_ref[...] = (ql * _rsq(ql) * qg_ref[...]).astype(qn_ref.dtype)
        kvn_ref[...] = (kv * _rsq(kv) * kg_ref[...]).astype(kvn_ref.dtype)

    return _pcall(body, name=name, grid=(T // tm,),
                  in_specs=[_row_spec(tm, LAT_W), pl.BlockSpec((1, QL), lambda i: (0, 0)), pl.BlockSpec((1, KVL), lambda i: (0, 0))],
                  out_specs=[_row_spec(tm, QL), _row_spec(tm, KVL)],
                  out_shape=[_sds((T, QL), MX), _sds((T, KVL), MX)])(z, q_gain, kv_gain)


def _lat_bwd(z, dqn, dkvn, q_gain, kv_gain, *, nb, name, tm=256):
    T = z.shape[0]
    tps = (T // nb) // tm

    def norm_bwd(x, dy, gain):
        r = _rsq(x)
        xhat = x * r
        dgain = jnp.sum(dy * xhat, axis=0, keepdims=True)
        dxhat = dy * gain
        return r * (dxhat - xhat * jnp.mean(dxhat * xhat, axis=-1, keepdims=True)), dgain

    def body(z_ref, dq_ref, dkv_ref, qg_ref, kg_ref, dql_ref, dkvl_ref, sq_ref, sk_ref):
        i = pl.program_id(0)
        dql, dqg = norm_bwd(z_ref[:, Z_QL:Z_QL + QL], dq_ref[...], qg_ref[...])
        dkvl, dkg = norm_bwd(z_ref[:, Z_KV:Z_KV + KVL], dkv_ref[...], kg_ref[...])
        dql_ref[...] = dql.astype(dql_ref.dtype)
        dkvl_ref[...] = dkvl.astype(dkvl_ref.dtype)

        @pl.when(i % tps == 0)
        def _():
            sq_ref[...] = jnp.zeros_like(sq_ref)
            sk_ref[...] = jnp.zeros_like(sk_ref)

        sq_ref[0, 0:1, :] += dqg
        sk_ref[0, 0:1, :] += dkg

    return _pcall(body, name=name, grid=(T // tm,),
                  in_specs=[_row_spec(tm, LAT_W), _row_spec(tm, QL), _row_spec(tm, KVL),
                            pl.BlockSpec((1, QL), lambda i: (0, 0)), pl.BlockSpec((1, KVL), lambda i: (0, 0))],
                  out_specs=[_row_spec(tm, QL), _row_spec(tm, KVL),
                             pl.BlockSpec((1, SUBLANE, QL), lambda i: (i // tps, 0, 0)),
                             pl.BlockSpec((1, SUBLANE, KVL), lambda i: (i // tps, 0, 0))],
                  out_shape=[_sds((T, QL), MX), _sds((T, KVL), MX), _sds((nb, SUBLANE, QL), F32), _sds((nb, SUBLANE, KVL), F32)],
                  )(z, dqn, dkvn, q_gain, kv_gain)


def _lane_masks(shape):
    lane = lax.broadcasted_iota(jnp.int32, shape, len(shape) - 1)
    m_n = lane < NOPE
    m_r = jnp.logical_and(lane >= KR_LANE, lane < KR_LANE + ROPE)
    first_half = lane < KR_LANE + ROPE // 2
    return m_n, m_r, first_half


def _rot(y, first_half):
    return jnp.where(first_half, -pltpu.roll(y, LANE - ROPE // 2, 1), pltpu.roll(y, ROPE // 2, 1))


def _rot_t(v, first_half, m_r):
    return jnp.where(m_r, jnp.where(first_half, pltpu.roll(v, LANE - ROPE // 2, 1), -pltpu.roll(v, ROPE // 2, 1)), 0.0)


def _prep_fwd(qp, kvp, z, cos, sin, q_gain, k_gain, *, name, tm=256):
    T = qp.shape[0]
    slab = pl.BlockSpec((tm, LANE), lambda i: (i, 0))
    kr_spec = pl.BlockSpec((tm, LANE), lambda i: (i, Z_KR // LANE))
    vec = pl.BlockSpec((1, LANE), lambda i: (0, 0))

    def body(qp_ref, kvp_ref, kr_ref, cos_ref, sin_ref, qg_ref, kg_ref, q_ref, k_ref, v_ref):
        m_n, m_r, first_half = _lane_masks((tm, LANE))
        c = cos_ref[...]
        s = sin_ref[...]
        qg = qg_ref[...]
        kg = kg_ref[...]
        xr = kr_ref[...]
        rr = lax.rsqrt(jnp.sum(xr * xr, axis=-1, keepdims=True) * (1.0 / ROPE) + EPS)
        yr = xr * rr * kg
        kr = jnp.where(m_r, yr * c + _rot(yr, first_half) * s, 0.0)
        for h in range(NH):
            x = qp_ref[:, h * LANE:(h + 1) * LANE]
            x2 = x * x
            rn = lax.rsqrt(jnp.sum(jnp.where(m_n, x2, 0.0), axis=-1, keepdims=True) * (1.0 / NOPE) + EPS)
            rq = lax.rsqrt(jnp.sum(jnp.where(m_r, x2, 0.0), axis=-1, keepdims=True) * (1.0 / ROPE) + EPS)
            y = x * jnp.where(m_n, rn, jnp.where(m_r, rq, 0.0)) * qg
            q_ref[:, h * LANE:(h + 1) * LANE] = (y * c + _rot(y, first_half) * s).astype(q_ref.dtype)
            xk = kvp_ref[:, h * LANE:(h + 1) * LANE]
            rk = lax.rsqrt(jnp.sum(jnp.where(m_n, xk * xk, 0.0), axis=-1, keepdims=True) * (1.0 / NOPE) + EPS)
            k_ref[:, h * LANE:(h + 1) * LANE] = (jnp.where(m_n, xk * rk * kg, 0.0) + kr).astype(k_ref.dtype)
        v_ref[...] = kvp_ref[:, NH * LANE:].astype(v_ref.dtype)

    return _pcall(body, name=name, grid=(T // tm,),
                  in_specs=[_row_spec(tm, NH * LANE), _row_spec(tm, NH * LANE + NH * NOPE), kr_spec, slab, slab, vec, vec],
                  out_specs=[_row_spec(tm, NH * LANE), _row_spec(tm, NH * LANE), _row_spec(tm, NH * NOPE)],
                  out_shape=[_sds((T, NH * LANE), MX), _sds((T, NH * LANE), MX), _sds((T, NH * NOPE), MX)],
                  )(qp, kvp, z, cos, sin, q_gain, k_gain)


def _prep_bwd(dq, dk, dv, qp, kvp, z, cos, sin, q_gain, k_gain, *, nb, name, tm=256):
    T = qp.shape[0]
    tps = (T // nb) // tm
    slab = pl.BlockSpec((tm, LANE), lambda i: (i, 0))
    kr_spec = pl.BlockSpec((tm, LANE), lambda i: (i, Z_KR // LANE))
    vec = pl.BlockSpec((1, LANE), lambda i: (0, 0))

    def body(dq_ref, dk_ref, dv_ref, qp_ref, kvp_ref, kr_ref, cos_ref, sin_ref, qg_ref, kg_ref,
             dqp_ref, dkvp_ref, dkr_ref, st_ref):
        i = pl.program_id(0)
        m_n, m_r, first_half = _lane_masks((tm, LANE))
        c = cos_ref[...]
        s = sin_ref[...]
        qg = qg_ref[...]
        kg = kg_ref[...]
        dqg = jnp.zeros((1, LANE), F32)
        dkg = jnp.zeros((1, LANE), F32)
        dkr_sum = jnp.zeros((tm, LANE), F32)
        for h in range(NH):
            x = qp_ref[:, h * LANE:(h + 1) * LANE]
            x2 = x * x
            rn = lax.rsqrt(jnp.sum(jnp.where(m_n, x2, 0.0), axis=-1, keepdims=True) * (1.0 / NOPE) + EPS)
            rq = lax.rsqrt(jnp.sum(jnp.where(m_r, x2, 0.0), axis=-1, keepdims=True) * (1.0 / ROPE) + EPS)
            rfac = jnp.where(m_n, rn, jnp.where(m_r, rq, 0.0))
            xhat = x * rfac
            do = dq_ref[:, h * LANE:(h + 1) * LANE]
            dy = do * c + _rot_t(do * s, first_half, m_r)
            dqg = dqg + jnp.sum(dy * xhat, axis=0, keepdims=True)
            dxhat = dy * qg
            t = dxhat * xhat
            mean_n = jnp.sum(jnp.where(m_n, t, 0.0), axis=-1, keepdims=True) * (1.0 / NOPE)
            mean_r = jnp.sum(jnp.where(m_r, t, 0.0), axis=-1, keepdims=True) * (1.0 / ROPE)
            dqp_ref[:, h * LANE:(h + 1) * LANE] = (
                rfac * (dxhat - xhat * jnp.where(m_n, mean_n, jnp.where(m_r, mean_r, 0.0)))).astype(dqp_ref.dtype)

            xk = kvp_ref[:, h * LANE:(h + 1) * LANE]
            rk = lax.rsqrt(jnp.sum(jnp.where(m_n, xk * xk, 0.0), axis=-1, keepdims=True) * (1.0 / NOPE) + EPS)
            khat = jnp.where(m_n, xk * rk, 0.0)
            dko = dk_ref[:, h * LANE:(h + 1) * LANE]
            dkn = jnp.where(m_n, dko, 0.0)
            dkg = dkg + jnp.sum(dkn * khat, axis=0, keepdims=True)
            dkhat = dkn * kg
            mean_k = jnp.sum(dkhat * khat, axis=-1, keepdims=True) * (1.0 / NOPE)
            dkvp_ref[:, h * LANE:(h + 1) * LANE] = jnp.where(m_n, rk * (dkhat - khat * mean_k), 0.0).astype(dkvp_ref.dtype)
            dkr_sum = dkr_sum + jnp.where(m_r, dko, 0.0)
        dkvp_ref[:, NH * LANE:] = dv_ref[...].astype(dkvp_ref.dtype)

        xr = kr_ref[...]
        rr = lax.rsqrt(jnp.sum(xr * xr, axis=-1, keepdims=True) * (1.0 / ROPE) + EPS)
        rhat = xr * rr
        dyr = dkr_sum * c + _rot_t(dkr_sum * s, first_half, m_r)
        dkg = dkg + jnp.sum(dyr * rhat, axis=0, keepdims=True)
        drhat = dyr * kg
        mean_kr = jnp.sum(drhat * rhat, axis=-1, keepdims=True) * (1.0 / ROPE)
        dkr_ref[...] = jnp.where(m_r, rr * (drhat - rhat * mean_kr), 0.0).astype(dkr_ref.dtype)

        @pl.when(i % tps == 0)
        def _():
            st_ref[...] = jnp.zeros_like(st_ref)

        st_ref[0, 0:1, :] += dqg
        st_ref[0, 1:2, :] += dkg

    return _pcall(body, name=name, grid=(T // tm,),
                  in_specs=[_row_spec(tm, NH * LANE), _row_spec(tm, NH * LANE), _row_spec(tm, NH * NOPE),
                            _row_spec(tm, NH * LANE), _row_spec(tm, NH * LANE + NH * NOPE), kr_spec, slab, slab, vec, vec],
                  out_specs=[_row_spec(tm, NH * LANE), _row_spec(tm, NH * LANE + NH * NOPE), slab,
                             pl.BlockSpec((1, SUBLANE, LANE), lambda i: (i // tps, 0, 0))],
                  out_shape=[_sds((T, NH * LANE), MX), _sds((T, NH * LANE + NH * NOPE), MX), _sds((T, LANE), MX),
                             _sds((nb, SUBLANE, LANE), F32)],
                  )(dq, dk, dv, qp, kvp, z, cos, sin, q_gain, k_gain)


def _lower_triangle(t):
    return lax.broadcasted_iota(jnp.int32, (t, t), 1) <= lax.broadcasted_iota(jnp.int32, (t, t), 0)


def _attn_fwd(q, k, v, *, nb, name, tq=512, comm=None):
    T = q.shape[0]
    S = T // nb
    tq = _pick(S, tq)
    nq = S // tq
    tk = tq
    npair = NH // 2

    def body(q_ref, k_ref, v_ref, o_ref, lse_ref):
        qi = pl.program_id(2)
        lane = lax.broadcasted_iota(jnp.int32, (tq, LANE), 1)
        qs = [q_ref[:, hh * LANE:(hh + 1) * LANE] for hh in range(2)]

        def block(j, carry, diagonal):
            k0 = pl.multiple_of(j * tk, tk)
            vb = v_ref[pl.ds(k0, tk), :]
            new = []
            for hh in range(2):
                m, l, acc = carry[hh]
                kb = k_ref[pl.ds(k0, tk), hh * LANE:(hh + 1) * LANE]
                s = lax.dot_general(qs[hh], kb, (((1,), (1,)), ((), ())), preferred_element_type=F32) * ATTN_SCALE
                if diagonal:
                    s = jnp.where(_lower_triangle(tq), s, NEG)
                m_new = jnp.maximum(m, jnp.max(s, axis=-1, keepdims=True))
                p = jnp.exp(s - m_new)
                alpha = jnp.exp(m - m_new)
                l = alpha * l + jnp.sum(p, axis=-1, keepdims=True)
                acc = alpha * acc + jnp.dot(p.astype(MX), vb, preferred_element_type=F32)
                new.append((m_new, l, acc))
            return tuple(new)

        init = tuple((jnp.full((tq, 1), NEG, F32), jnp.zeros((tq, 1), F32), jnp.zeros((tq, LANE), F32)) for _ in range(2))
        carry = lax.fori_loop(0, qi, lambda j, c: block(j, c, False), init)
        (m0, l0, acc0), (m1, l1, acc1) = block(qi, carry, True)
        o_ref[...] = jnp.where(lane < NOPE, acc0 / l0, acc1 / l1).astype(o_ref.dtype)
        lse_ref[...] = jnp.where(lane < NOPE, m0 + jnp.log(l0), m1 + jnp.log(l1))

    return _pcall(body, name=name, grid=(nb, npair, nq),
                  in_specs=[pl.BlockSpec((tq, 2 * LANE), lambda b, p, i: (b * nq + i, p)),
                            pl.BlockSpec((S, 2 * LANE), lambda b, p, i: (b, p)),
                            pl.BlockSpec((S, LANE), lambda b, p, i: (b, p))],
                  out_specs=[pl.BlockSpec((tq, LANE), lambda b, p, i: (b * nq + i, p)),
                             pl.BlockSpec((tq, LANE), lambda b, p, i: (b * nq + i, p))],
                  out_shape=[_sds((T, NH * NOPE), MX), _sds((T, NH * NOPE), F32)], comm=comm)(q, k, v)


def _attn_bwd(q, k, v, o, lse, do, *, nb, name, tq=512, comm=None):
    T = q.shape[0]
    S = T // nb
    tq = _pick(S, tq)
    nq = S // tq
    tk = tq
    npair = NH // 2

    def body(q_ref, k_ref, v_ref, o_ref, lse_ref, do_ref, dq_ref, dk_ref, dv_ref, delta_ref):
        lane = lax.broadcasted_iota(jnp.int32, (tq, LANE), 1)
        first = lane < NOPE
        dq_ref[...] = jnp.zeros_like(dq_ref)

        def delta_step(qi, _):
            q0 = pl.multiple_of(qi * tq, tq)
            prod = do_ref[pl.ds(q0, tq), :] * o_ref[pl.ds(q0, tq), :].astype(F32)
            d0 = jnp.sum(jnp.where(first, prod, 0.0), axis=-1, keepdims=True)
            d1 = jnp.sum(jnp.where(first, 0.0, prod), axis=-1, keepdims=True)
            delta_ref[pl.ds(q0, tq), :] = jnp.where(first, d0, d1)
            return 0

        lax.fori_loop(0, nq, delta_step, 0)

        def kv_step(kj, _):
            k0 = pl.multiple_of(kj * tk, tk)
            kbs = [k_ref[pl.ds(k0, tk), hh * LANE:(hh + 1) * LANE] for hh in range(2)]
            vb = v_ref[pl.ds(k0, tk), :]

            def q_block(qi, carry, diagonal):
                dk0, dk1, dv = carry
                dks = [dk0, dk1]
                q0 = pl.multiple_of(qi * tq, tq)
                dov = do_ref[pl.ds(q0, tq), :]
                lse_v = lse_ref[pl.ds(q0, tq), :]
                delta_v = delta_ref[pl.ds(q0, tq), :]
                for hh in range(2):
                    qh = q_ref[pl.ds(q0, tq), hh * LANE:(hh + 1) * LANE]
                    dob = jnp.where(first if hh == 0 else jnp.logical_not(first), dov, 0.0).astype(MX)
                    s = lax.dot_general(qh, kbs[hh], (((1,), (1,)), ((), ())), preferred_element_type=F32) * ATTN_SCALE
                    p = jnp.exp(s - lse_v[:, hh * NOPE:hh * NOPE + 1])
                    if diagonal:
                        p = jnp.where(_lower_triangle(tq), p, 0.0)
                    dp = lax.dot_general(dob, vb, (((1,), (1,)), ((), ())), preferred_element_type=F32)
                    ds = (p * (dp - delta_v[:, hh * NOPE:hh * NOPE + 1]) * ATTN_SCALE).astype(MX)
                    dks[hh] = dks[hh] + lax.dot_general(ds, qh, (((0,), (0,)), ((), ())), preferred_element_type=F32)
                    dv = dv + lax.dot_general(p.astype(MX), dob, (((0,), (0,)), ((), ())), preferred_element_type=F32)
                    dq_ref[pl.ds(q0, tq), hh * LANE:(hh + 1) * LANE] += jnp.dot(ds, kbs[hh], preferred_element_type=F32)
                return dks[0], dks[1], dv

            zero = jnp.zeros((tk, LANE), F32)
            carry = q_block(kj, (zero, zero, zero), True)
            dk0, dk1, dv = lax.fori_loop(kj + 1, nq, lambda qi, c: q_block(qi, c, False), carry)
            dk_ref[pl.ds(k0, tk), 0:LANE] = dk0
            dk_ref[pl.ds(k0, tk), LANE:2 * LANE] = dk1
            dv_ref[pl.ds(k0, tk), :] = dv
            return 0

        lax.fori_loop(0, nq, kv_step, 0)

    pair256 = pl.BlockSpec((S, 2 * LANE), lambda b, p: (b, p))
    pair128 = pl.BlockSpec((S, LANE), lambda b, p: (b, p))
    return _pcall(body, name=name, grid=(nb, npair),
                  in_specs=[pair256, pair256, pair128, pair128, pair128, pair128],
                  out_specs=[pair256, pair256, pair128],
                  out_shape=[_sds((T, NH * LANE), F32), _sds((T, NH * LANE), F32), _sds((T, NH * NOPE), F32)],
                  scratch=[pltpu.VMEM((S, LANE), F32)], comm=comm)(q, k, v, o, lse, do)


class _NoExchange:
    def __init__(self, weights):
        self.W, self.G = weights, {}

    def rider(self, name):
        return None

    def landed(self, name, outs):
        pass

    def grad(self, key, g):
        self.G[key] = g


def _run(plan, fn, *args, name, **kw):
    rider = plan.rider(name)
    if rider is None:
        return fn(*args, name=name, **kw)
    outs, landed = fn(*args, name=name, comm=rider, **kw)
    plan.landed(name, landed)
    return outs


def _layer_fwd_bwd(x, tgt, mod3, cos, sin, plan, P):
    nb = mod3.shape[0]
    W = plan.W
    h1 = _fwd_block(x, None, mod3, P["norm_ffn1"], sub=0, coef=0.0, name="fwd_norm1")
    gu1, a1 = _run(plan, _ffn_up, h1, W["ffn1_in"], name="ffn1_up")
    f1 = _mm(a1, W["ffn1_out"], mode="nn", out_dtype=F32, name="ffn1_out", tm=1024, tn=1024, tk=DFF)
    x1, h2 = _fwd_block(x, f1, mod3, P["norm_mix"], sub=1, coef=0.5, name="fwd_norm2")
    z = _mm(h2, W["w_in"], mode="nn", out_dtype=F32, name="mix_in", tn=1664)
    pooled, mixed, scaled = _pool_fwd(z, W["pool_grp"], P["pool_scale"], nb=nb, name="pool_fwd")
    br_pool = _mm(scaled, W["pool_proj"], mode="nn", out_dtype=F32, name="pool_proj")
    qn, kvn = _lat_fwd(z, P["q_a_norm"], P["kv_a_norm"], name="lat_fwd")
    qp = _mm(qn, W["q_up"], mode="nn", out_dtype=F32, name="q_up")
    kvp = _mm(kvn, W["kv_up"], mode="nn", out_dtype=F32, name="kv_up")
    q, k, v = _prep_fwd(qp, kvp, z, cos, sin, P["q_gain"], P["k_gain"], name="prep_fwd")
    attn, lse = _run(plan, _attn_fwd, q, k, v, nb=nb, name="attn_fwd")
    br_mla = _mm(attn, W["mla_proj"], mode="nn", out_dtype=F32, name="mla_proj")
    merged = _merge_fwd(z, br_pool, br_mla, name="merge_fwd")
    mo = _mm(merged, W["w_out"], mode="nn", out_dtype=F32, name="mix_out")
    x2, h3 = _fwd_block(x1, mo, mod3, P["norm_ffn2"], sub=2, coef=1.0, name="fwd_norm3")
    gu2, a2 = _ffn_up(h3, W["ffn2_in"], name="ffn2_up")
    f2 = _mm(a2, W["ffn2_out"], mode="nn", out_dtype=F32, name="ffn2_out", tm=1024, tn=1024, tk=DFF)
    dy, df2, st_fin, loss = _final(x2, f2, tgt, mod3, name="loss_head")

    tokens = x.shape[0]
    plan.grad("ffn2_out", _mm(a2, df2, mode="tn", out_dtype=F32, name="d_ffn2_out", tm=256, tn=D, tk=tokens, n_outer=True))
    dgu2 = _ffn_dact(df2, gu2, W["ffn2_out"], name="ffn2_dact")
    dh3 = _ffn_dh(dgu2, W["ffn2_in"], name="d_ffn2_h")
    plan.grad("ffn2_in", _ffn_dw_in(h3, dgu2, name="d_ffn2_in"))
    dx2, dmo, st3 = _run(plan, _bwd_block, x2, dh3, dy, mo, mod3, P["norm_ffn2"], sub=2, coef=1.0, name="bwd_norm3")
    plan.grad("w_out", _mm(merged, dmo, mode="tn", out_dtype=F32, name="d_mix_out"))
    dmerged = _mm(dmo, W["w_out"], mode="nt", out_dtype=F32, name="d_merged")
    dbr_pool, dbr_mla, dgates = _merge_bwd(z, br_pool, br_mla, dmerged, name="merge_bwd")
    plan.grad("pool_proj", _mm(scaled, dbr_pool, mode="tn", out_dtype=F32, name="d_pool_proj"))
    dscaled = _mm(dbr_pool, W["pool_proj"], mode="nt", out_dtype=F32, name="d_pool_scaled")
    du_pool, d_pool_grp, d_pool_scale = _pool_bwd(dscaled, mixed, pooled, W["pool_grp"], P["pool_scale"], nb=nb, name="pool_bwd")
    plan.grad("mla_proj", _mm(attn, dbr_mla, mode="tn", out_dtype=F32, name="d_mla_proj"))
    dattn = _mm(dbr_mla, W["mla_proj"], mode="nt", out_dtype=F32, name="d_attn")
    dq, dk, dv = _run(plan, _attn_bwd, q, k, v, attn, lse, dattn, nb=nb, name="attn_bwd")
    dqp, dkvp, dkr, st_prep = _prep_bwd(dq, dk, dv, qp, kvp, z, cos, sin, P["q_gain"], P["k_gain"], nb=nb, name="prep_bwd")
    plan.grad("q_up", _mm(qn, dqp, mode="tn", out_dtype=F32, name="d_q_up"))
    dqn = _mm(dqp, W["q_up"], mode="nt", out_dtype=F32, name="d_qn")
    plan.grad("kv_up", _mm(kvn, dkvp, mode="tn", out_dtype=F32, name="d_kv_up"))
    dkvn = _mm(dkvp, W["kv_up"], mode="nt", out_dtype=F32, name="d_kvn", tk=1536)
    dql, dkvl, st_q, st_kv = _lat_bwd(z, dqn, dkvn, P["q_a_norm"], P["kv_a_norm"], nb=nb, name="lat_bwd")
    dz = jnp.concatenate([du_pool, dql, dkvl, dkr, dgates], axis=1)
    plan.grad("w_in", _mm(h2, dz, mode="tn", out_dtype=F32, name="d_mix_in", tn=1664))
    dh2 = _run(plan, _mm, dz, W["w_in"], mode="nt", out_dtype=F32, name="d_mix_h", tk=1664)
    dx1, df1, st2 = _bwd_block(x1, dh2, dx2, f1, mod3, P["norm_mix"], sub=1, coef=0.5, name="bwd_norm2")
    plan.grad("ffn1_out", _mm(a1, df1, mode="tn", out_dtype=F32, name="d_ffn1_out", tm=256, tn=D, tk=tokens, n_outer=True))
    dgu1 = _run(plan, _ffn_dact, df1, gu1, W["ffn1_out"], name="ffn1_dact")
    dh1 = _run(plan, _ffn_dh, dgu1, W["ffn1_in"], name="d_ffn1_h")
    plan.grad("ffn1_in", _run(plan, _ffn_dw_in, h1, dgu1, name="d_ffn1_in"))
    grad_x, st1 = _run(plan, _bwd_block, x, dh1, dx1, None, mod3, P["norm_ffn1"], sub=0, coef=0.0, name="bwd_norm1")

    dmod = jnp.stack([st1[:, 0], st1[:, 1], st2[:, 3], st2[:, 0], st2[:, 1], st3[:, 3], st3[:, 0], st3[:, 1], st_fin[:, 0]],
                     axis=1)
    small = {
        "norm_ffn1": st1[:, 2], "norm_mix": st2[:, 2], "norm_ffn2": st3[:, 2],
        "pool_grp": d_pool_grp, "pool_scale": d_pool_scale[:, 0],
        "q_a_norm": st_q[:, 0], "kv_a_norm": st_kv[:, 0],
        "q_gain": st_prep[:, 0], "k_gain": st_prep[:, 1],
    }
    return loss, grad_x, small, dmod


def _w_in_to_kernel(w):
    k = w.shape[0]
    zeros = lambda n: jnp.zeros((k, n), w.dtype)
    return jnp.concatenate([w[:, 0:1152], zeros(KR_LANE), w[:, 1152:1184], zeros(LANE - KR_LANE - ROPE), w[:, 1184:]], axis=1)


def _w_in_from_kernel(g):
    return jnp.concatenate([g[:, 0:1152], g[:, Z_KR + KR_LANE:Z_KR + KR_LANE + ROPE], g[:, Z_GP:]], axis=1)


def _q_up_to_kernel(w):
    k = w.shape[0]
    return jnp.pad(w.reshape(k, NH, NOPE + ROPE), ((0, 0), (0, 0), (0, LANE - NOPE - ROPE))).reshape(k, NH * LANE)


def _q_up_from_kernel(g):
    k = g.shape[0]
    return g.reshape(k, NH, LANE)[:, :, :NOPE + ROPE].reshape(k, NH * (NOPE + ROPE))


def _kv_up_to_kernel(w):
    k = w.shape[0]
    w3 = w.reshape(k, NH, 2 * NOPE)
    kpart = jnp.pad(w3[:, :, :NOPE], ((0, 0), (0, 0), (0, LANE - NOPE))).reshape(k, NH * LANE)
    return jnp.concatenate([kpart, w3[:, :, NOPE:].reshape(k, NH * NOPE)], axis=1)


def _kv_up_from_kernel(g):
    k = g.shape[0]
    kpart = g[:, :NH * LANE].reshape(k, NH, LANE)[:, :, :NOPE]
    vpart = g[:, NH * LANE:].reshape(k, NH, NOPE)
    return jnp.concatenate([kpart, vpart], axis=2).reshape(k, NH * 2 * NOPE)


def _gain_slab(nope, rope):
    return jnp.concatenate([nope, rope, jnp.zeros((1, LANE - NOPE - ROPE), nope.dtype)], axis=1)


def _rope_tables(positions):
    inv_freq = 10000.0 ** (-jnp.arange(0, ROPE, 2, dtype=F32) / ROPE)
    ang = positions.astype(F32)[:, None] * inv_freq
    ang = jnp.concatenate([ang, ang], axis=-1)
    t = positions.shape[0]
    cos = jnp.concatenate([jnp.ones((t, KR_LANE), F32), jnp.cos(ang), jnp.ones((t, LANE - KR_LANE - ROPE), F32)], axis=1)
    sin = jnp.concatenate([jnp.zeros((t, KR_LANE), F32), jnp.sin(ang), jnp.zeros((t, LANE - KR_LANE - ROPE), F32)], axis=1)
    return cos, sin


def _coords():
    return lax.axis_index("x"), lax.axis_index("y"), lax.axis_index("c")


HBM_SPEC = pl.BlockSpec(memory_space=pl.ANY)
VMEM_SPEC = pl.BlockSpec(memory_space=pltpu.VMEM)


def _gather8(v, *, name, comm=None):
    r, c = v.shape

    def body(v_ref, out_ref, send_sems, recv_sems, local_sem):
        x, y, cc = _coords()
        me = 4 * x + 2 * y + cc
        mine = pltpu.make_async_copy(v_ref, out_ref.at[me], local_sem)
        mine.start()
        copies = []
        for kk in range(1, N_DEV):
            peer = (x ^ (kk >> 2), y ^ ((kk >> 1) & 1), cc ^ (kk & 1))
            cp = pltpu.make_async_remote_copy(src_ref=v_ref, dst_ref=out_ref.at[me], send_sem=send_sems.at[kk - 1],
                                              recv_sem=recv_sems.at[kk - 1], device_id=peer, device_id_type=MESH)
            cp.start()
            copies.append(cp)
        for kk in range(1, N_DEV):
            peer_slot = me ^ kk
            pltpu.make_async_remote_copy(src_ref=v_ref, dst_ref=out_ref.at[peer_slot], send_sem=send_sems.at[kk - 1],
                                         recv_sem=recv_sems.at[kk - 1], device_id=(x, y, cc), device_id_type=MESH).wait_recv()
        for cp in copies:
            cp.wait_send()
        mine.wait()

    return _pcall(body, name=name, out_shape=_sds((N_DEV, r, c), v.dtype), in_specs=[VMEM_SPEC], out_specs=VMEM_SPEC,
                  scratch=[pltpu.SemaphoreType.DMA((N_DEV - 1,)), pltpu.SemaphoreType.DMA((N_DEV - 1,)), pltpu.SemaphoreType.DMA],
                  comm=comm)(v)


CHIP_RELS = ((1, 0), (0, 1), (1, 1))


def _comm_only(comm, *, name):
    _, outs = _pcall(lambda: None, name=name, out_shape=[], in_specs=[], out_specs=[], comm=comm)()
    return outs


def _gather_comm(shards):
    n = len(shards)

    def ici(ins, outs, sems, a, j, x, y, cc):
        half = ins[a].shape[0] // 2
        mine = pl.ds(cc * half, half)
        dx, dy = CHIP_RELS[j]
        return pltpu.make_async_remote_copy(src_ref=ins[a].at[mine], dst_ref=outs[a].at[2 * x + y, mine],
                                            send_sem=sems[0].at[a, j], recv_sem=sems[1].at[a, j],
                                            device_id=(x ^ dx, y ^ dy, cc), device_id_type=MESH)

    def d2d(ins, outs, sems, a, j, x, y, cc, half_of):
        half = ins[a].shape[0] // 2
        dx, dy = CHIP_RELS[j]
        landed = outs[a].at[2 * (x ^ dx) + (y ^ dy), pl.ds(half_of * half, half)]
        return pltpu.make_async_remote_copy(src_ref=landed, dst_ref=landed, send_sem=sems[2].at[a, j], recv_sem=sems[3].at[a, j],
                                            device_id=(x, y, 1 - cc), device_id_type=MESH)

    def own(ins, outs, sems, a, x, y):
        return pltpu.make_async_copy(ins[a], outs[a].at[2 * x + y], sems[4].at[a])

    def start(ins, outs, sems):
        x, y, cc = _coords()
        for a in range(n):
            own(ins, outs, sems, a, x, y).start()
            for j in range(3):
                ici(ins, outs, sems, a, j, x, y, cc).start()

    def finish(ins, outs, sems):
        x, y, cc = _coords()
        for a in range(n):
            for j in range(3):
                ici(ins, outs, sems, a, j, x, y, cc).wait_recv()
                d2d(ins, outs, sems, a, j, x, y, cc, cc).start()
        for a in range(n):
            for j in range(3):
                d2d(ins, outs, sems, a, j, x, y, cc, 1 - cc).wait_recv()
        for a in range(n):
            for j in range(3):
                ici(ins, outs, sems, a, j, x, y, cc).wait_send()
                d2d(ins, outs, sems, a, j, x, y, cc, cc).wait_send()
            own(ins, outs, sems, a, x, y).wait()

    dma = pltpu.SemaphoreType.DMA
    return _Comm(shards, [_sds((N_CHIP,) + s.shape, s.dtype) for s in shards],
                 [dma((n, 3)), dma((n, 3)), dma((n, 3)), dma((n, 3)), dma((n,))], start, finish)


def _swap_comm(parts):
    n = len(parts)

    def copy(ins, outs, sems, a):
        x, y, cc = _coords()
        half = ins[a].shape[1] // 2
        return pltpu.make_async_remote_copy(src_ref=ins[a].at[:, pl.ds((1 - cc) * half, half)], dst_ref=outs[a],
                                            send_sem=sems[0].at[a], recv_sem=sems[1].at[a], device_id=(x, y, 1 - cc),
                                            device_id_type=MESH)

    def start(ins, outs, sems):
        for a in range(n):
            copy(ins, outs, sems, a).start()

    def finish(ins, outs, sems):
        for a in range(n):
            copy(ins, outs, sems, a).wait()

    dma = pltpu.SemaphoreType.DMA
    return _Comm(parts, [_sds((p.shape[0], p.shape[1] // 2, p.shape[2]), p.dtype) for p in parts], [dma((n,)), dma((n,))],
                 start, finish)


def _add_half(full, other, cidx, *, name):
    nch, r, c = full.shape
    half = r // 2
    tr = _pick_rows(half)
    nbk = half // tr
    grid_spec = pltpu.PrefetchScalarGridSpec(
        num_scalar_prefetch=1, grid=(nch, nbk),
        in_specs=[pl.BlockSpec((1, tr, c), lambda j, i, cref: (j, cref[0] * nbk + i, 0)),
                  pl.BlockSpec((1, tr, c), lambda j, i, cref: (j, i, 0))],
        out_specs=pl.BlockSpec((1, tr, c), lambda j, i, cref: (j, i, 0)))

    def body(cref, a_ref, b_ref, o_ref):
        o_ref[...] = (a_ref[...] + b_ref[...]).astype(o_ref.dtype)

    return _pcall(body, name=name, out_shape=_sds((nch, half, c), MX), grid_spec=grid_spec)(cidx, full, other)


def _pick_rows(rows, target=512):
    best = None
    for t in range(16, min(rows, target) + 1, 16):
        if rows % t == 0:
            best = t
    return rows if best is None else best


def _exchange_comm(parts):
    n = len(parts)

    def send(ins, outs, sems, a, j, x, y, cc):
        dx, dy = CHIP_RELS[j]
        return pltpu.make_async_remote_copy(src_ref=ins[a].at[2 * (x ^ dx) + (y ^ dy)], dst_ref=outs[a].at[2 * x + y],
                                            send_sem=sems[0].at[a, j], recv_sem=sems[1].at[a, j],
                                            device_id=(x ^ dx, y ^ dy, cc), device_id_type=MESH)

    def landing(ins, outs, sems, a, j, x, y, cc):
        dx, dy = CHIP_RELS[j]
        peer_chip = 2 * (x ^ dx) + (y ^ dy)
        return pltpu.make_async_remote_copy(src_ref=ins[a].at[peer_chip], dst_ref=outs[a].at[peer_chip], send_sem=sems[0].at[a, j],
                                            recv_sem=sems[1].at[a, j], device_id=(x, y, cc), device_id_type=MESH)

    def own(ins, outs, sems, a, x, y):
        return pltpu.make_async_copy(ins[a].at[2 * x + y], outs[a].at[2 * x + y], sems[2].at[a])

    def start(ins, outs, sems):
        x, y, cc = _coords()
        for a in range(n):
            own(ins, outs, sems, a, x, y).start()
            for j in range(3):
                send(ins, outs, sems, a, j, x, y, cc).start()

    def finish(ins, outs, sems):
        x, y, cc = _coords()
        for a in range(n):
            for j in range(3):
                landing(ins, outs, sems, a, j, x, y, cc).wait_recv()
        for a in range(n):
            for j in range(3):
                send(ins, outs, sems, a, j, x, y, cc).wait_send()
            own(ins, outs, sems, a, x, y).wait()

    dma = pltpu.SemaphoreType.DMA
    return _Comm(parts, [_sds(p.shape, p.dtype) for p in parts], [dma((n, 3)), dma((n, 3)), dma((n,))], start, finish)


def _sum_chips(q, cidx, *, name):
    nch, h, c = q.shape
    tr = _pick_rows(h)
    nbk = h // tr
    grid_spec = pltpu.PrefetchScalarGridSpec(
        num_scalar_prefetch=1, grid=(nbk,),
        in_specs=[pl.BlockSpec((nch, tr, c), lambda i, cref: (0, i, 0))],
        out_specs=pl.BlockSpec((tr, c), lambda i, cref: (cref[0] * nbk + i, 0)))

    def body(cref, q_ref, o_ref):
        acc = q_ref[0].astype(F32) + q_ref[1].astype(F32)
        acc = acc + q_ref[2].astype(F32)
        o_ref[...] = acc + q_ref[3].astype(F32)

    return _pcall(body, name=name, out_shape=_sds((2 * h, c), F32), grid_spec=grid_spec)(cidx, q)


def _join_comm(fulls):
    n = len(fulls)

    def half_copy(outs, sems, a, which):
        x, y, cc = _coords()
        h = outs[a].shape[0] // 2
        rows = outs[a].at[pl.ds((cc if which == 0 else 1 - cc) * h, h)]
        return pltpu.make_async_remote_copy(src_ref=rows, dst_ref=rows, send_sem=sems[0].at[a], recv_sem=sems[1].at[a],
                                            device_id=(x, y, 1 - cc), device_id_type=MESH)

    def start(ins, outs, sems):
        for a in range(n):
            half_copy(outs, sems, a, 0).start()

    def finish(ins, outs, sems):
        for a in range(n):
            half_copy(outs, sems, a, 1).wait_recv()
        for a in range(n):
            half_copy(outs, sems, a, 0).wait_send()

    dma = pltpu.SemaphoreType.DMA
    return _Comm(fulls, [_sds(p.shape, p.dtype) for p in fulls], [dma((n,)), dma((n,))], start, finish,
                 aliases={a: a for a in range(n)})


def _ada_fwd(c_all, w, b, *, name):
    m, kdim = c_all.shape
    n = w.shape[1]
    tn = _pick(n, 1152)

    def body(c_ref, w_ref, b_ref, o_ref):
        cv = c_ref[...]
        act = (cv * _sigmoid(cv)).astype(MX)
        o_ref[...] = jnp.dot(act, w_ref[...].astype(MX), preferred_element_type=F32) + b_ref[...]

    return _pcall(body, name=name, out_shape=_sds((m, n), F32), grid=(n // tn,),
                  in_specs=[pl.BlockSpec((m, kdim), lambda j: (0, 0)), pl.BlockSpec((kdim, tn), lambda j: (0, j)),
                            pl.BlockSpec((1, tn), lambda j: (0, j))],
                  out_specs=pl.BlockSpec((m, tn), lambda j: (0, j)))(c_all, w, b)


def _ada_bwd(c_all, dmod_cols, *, name):
    m, kdim = c_all.shape
    n = dmod_cols.shape[1]
    tn = _pick(n, 1152)

    def body(c_ref, d_ref, o_ref):
        cv = c_ref[...]
        act = (cv * _sigmoid(cv)).astype(MX)
        o_ref[...] = lax.dot_general(act, d_ref[...].astype(MX), (((0,), (0,)), ((), ())), preferred_element_type=F32)

    return _pcall(body, name=name, out_shape=_sds((kdim, n), F32), grid=(n // tn,),
                  in_specs=[pl.BlockSpec((m, kdim), lambda j: (0, 0)), pl.BlockSpec((m, tn), lambda j: (0, j))],
                  out_specs=pl.BlockSpec((kdim, tn), lambda j: (0, j)))(c_all, dmod_cols)


def _sum_small(v, nseq, rows, *, name):
    n, r, c = v.shape
    extra = r - nseq * rows

    def body(v_ref, o_ref):
        acc = None
        for d in range(n):
            for s in range(nseq):
                t = v_ref[d, s * rows:(s + 1) * rows, :]
                acc = t if acc is None else acc + t
        o_ref[0:rows, :] = acc
        acc = v_ref[0, nseq * rows:, :]
        for d in range(1, n):
            acc = acc + v_ref[d, nseq * rows:, :]
        o_ref[rows:, :] = acc

    return _pcall(body, name=name, out_shape=_sds((rows + extra, c), F32), in_specs=[VMEM_SPEC], out_specs=VMEM_SPEC)(v)


def _adamw(w, g, m, v, *, name, comm=None):
    r, c = w.shape
    tr = _pick_rows(r, 256)
    spec = pl.BlockSpec((tr, c), lambda i: (i, 0))
    bc1 = 1.0 / (1.0 - ADAM_B1 ** ADAM_STEP)
    bc2 = 1.0 / (1.0 - ADAM_B2 ** ADAM_STEP)

    def body(w_ref, g_ref, m_ref, v_ref, d_ref, mo_ref, vo_ref):
        gv = g_ref[...]
        mn = ADAM_B1 * m_ref[...] + (1.0 - ADAM_B1) * gv
        vn = ADAM_B2 * v_ref[...] + (1.0 - ADAM_B2) * (gv * gv)
        mo_ref[...] = mn
        vo_ref[...] = vn
        d_ref[...] = -ADAM_LR * ((mn * bc1) / (jnp.sqrt(vn * bc2) + ADAM_EPS) + ADAM_WD * w_ref[...])

    out = _sds((r, c), F32)
    return _pcall(body, name=name, out_shape=[out, out, out], grid=(r // tr,), in_specs=[spec] * 4, out_specs=[spec] * 3,
                  comm=comm)(w, g, m, v)


BIG = ("w_ffn1_in", "w_ffn1_out", "w_in", "w_pool_proj", "w_q_up", "w_kv_up", "w_mla_proj", "w_out", "w_ffn2_in", "w_ffn2_out")
ROW_SHARDED = ("w_ffn1_out", "w_out", "w_ffn2_out")
KERNEL_NAME = {"w_ffn1_in": "ffn1_in", "w_ffn1_out": "ffn1_out", "w_in": "w_in", "w_pool_proj": "pool_proj", "w_q_up": "q_up",
               "w_kv_up": "kv_up", "w_mla_proj": "mla_proj", "w_out": "w_out", "w_ffn2_in": "ffn2_in", "w_ffn2_out": "ffn2_out"}
WEIGHTS = ("w_ada", "b_ada", "norm_ffn1", "w_ffn1_in", "w_ffn1_out", "norm_mix", "w_in", "pool_grp", "pool_scale", "w_pool_proj",
           "q_a_norm", "w_q_up", "kv_a_norm", "w_kv_up", "q_norm_nope", "q_norm_rope", "k_norm_nope", "k_norm_rope", "w_mla_proj",
           "w_out", "norm_ffn2", "w_ffn2_in", "w_ffn2_out")
SMALL = ("b_ada", "norm_ffn1", "norm_mix", "pool_grp", "pool_scale", "q_a_norm", "kv_a_norm", "q_norm_nope", "q_norm_rope",
         "k_norm_nope", "k_norm_rope", "norm_ffn2")
SMALL_SEQ = tuple(n for n in SMALL if n != "pool_grp")
SLAB_W = 1024


def _assemble(name, stacked):
    if name in ROW_SHARDED:
        return stacked.reshape(stacked.shape[0] * stacked.shape[1], stacked.shape[2])
    return jnp.transpose(stacked, (1, 0, 2)).reshape(stacked.shape[1], stacked.shape[0] * stacked.shape[2])


def _split(name, full):
    if name in ROW_SHARDED:
        return full.reshape(N_CHIP, full.shape[0] // N_CHIP, full.shape[1])
    return jnp.transpose(full.reshape(full.shape[0], N_CHIP, full.shape[1] // N_CHIP), (1, 0, 2))


GU = ("w_ffn1_in", "w_ffn2_in")


def _to_rows(v, width=SLAB_W):
    flat = v.reshape(-1)
    rows = -(-flat.shape[0] // width)
    return jnp.pad(flat, (0, rows * width - flat.shape[0])).reshape(rows, width)


def _pack_small(parts, names=SMALL):
    rows, spans, at = [], {}, 0
    for name in names:
        r = _to_rows(parts[name])
        spans[name] = (at, parts[name].size, parts[name].shape)
        rows.append(r)
        at += r.shape[0]
    pad = (-at) % SUBLANE
    if pad:
        rows.append(jnp.zeros((pad, SLAB_W), F32))
    return jnp.concatenate(rows, axis=0), spans


def _unpack_small(slab, spans):
    out = {}
    for name, (at, size, shape) in spans.items():
        nrow = -(-size // SLAB_W)
        out[name] = slab[at:at + nrow].reshape(-1)[:size].reshape(shape)
    return out


GATHER_ON = {"gather_ffn1": ("w_ffn1_in", "w_ffn1_out"),
             "ffn1_up": ("w_in", "w_pool_proj", "w_q_up", "w_kv_up", "w_mla_proj", "w_out"),
             "attn_fwd": ("w_ffn2_in", "w_ffn2_out")}
SWAP_ON = {"bwd_norm3": ("w_ffn2_out", "w_ffn2_in"),
           "d_mix_h": ("w_out", "w_pool_proj", "w_mla_proj", "w_q_up", "w_kv_up", "w_in"),
           "ffn1_dact": ("w_ffn1_out",),
           "bwd_norm1": ("w_ffn1_in",)}
EXCHANGE_ON = {"attn_bwd": "bwd_norm3", "d_ffn1_h": "d_mix_h", "d_ffn1_in": "ffn1_dact", "gather_small": "bwd_norm1"}


class _ExchangePlan:
    def __init__(self, shards, cidx):
        self.shards, self.cidx = shards, cidx
        self.W, self.G, self.parts, self.pre, self.reduced = {}, {}, {}, {}, {}

    def grad(self, key, g):
        self.G[key] = g

    def rider(self, name):
        if name in GATHER_ON:
            return _gather_comm([self.shards[n] for n in GATHER_ON[name]])
        if name in SWAP_ON:
            self.parts[name] = [self._stacked(n) for n in SWAP_ON[name]]
            return _swap_comm(self.parts[name])
        if name in EXCHANGE_ON:
            return _exchange_comm(self.pre[EXCHANGE_ON[name]])
        return None

    def landed(self, name, outs):
        if name in GATHER_ON:
            for n, g in zip(GATHER_ON[name], outs):
                self.W[KERNEL_NAME[n]] = self._to_kernel(n, g)
        elif name in SWAP_ON:
            self.pre[name] = [_add_half(p, o, self.cidx, name="rs_add_" + KERNEL_NAME[n])
                              for n, p, o in zip(SWAP_ON[name], self.parts[name], outs)]
        elif name in EXCHANGE_ON:
            for n, q in zip(SWAP_ON[EXCHANGE_ON[name]], outs):
                self.reduced[n] = _sum_chips(q, self.cidx, name="rs_sum_" + KERNEL_NAME[n])

    @staticmethod
    def _to_kernel(n, stacked):
        if n in GU:
            return stacked
        full = _assemble(n, stacked)
        return {"w_in": _w_in_to_kernel, "w_q_up": _q_up_to_kernel, "w_kv_up": _kv_up_to_kernel}.get(n, lambda w: w)(full)

    def _stacked(self, n):
        g = self.G[KERNEL_NAME[n]]
        if n in GU:
            return g
        full = {"w_in": _w_in_from_kernel, "w_q_up": _q_up_from_kernel, "w_kv_up": _kv_up_from_kernel}.get(n, lambda w: w)(g)
        return _split(n, full)


def kernel(x, c, positions, w_ada, b_ada, norm_ffn1, w_ffn1_in, w_ffn1_out, norm_mix, w_in, pool_grp, pool_scale, w_pool_proj, q_a_norm, w_q_up, kv_a_norm, w_kv_up, q_norm_nope, q_norm_rope, k_norm_nope, k_norm_rope, w_mla_proj, w_out, norm_ffn2, w_ffn2_in, w_ffn2_out, loss_target, m_w_ada, m_b_ada, m_norm_ffn1, m_w_ffn1_in, m_w_ffn1_out, m_norm_mix, m_w_in, m_pool_grp, m_pool_scale, m_w_pool_proj, m_q_a_norm, m_w_q_up, m_kv_a_norm, m_w_kv_up, m_q_norm_nope, m_q_norm_rope, m_k_norm_nope, m_k_norm_rope, m_w_mla_proj, m_w_out, m_norm_ffn2, m_w_ffn2_in, m_w_ffn2_out, v_w_ada, v_b_ada, v_norm_ffn1, v_w_ffn1_in, v_w_ffn1_out, v_norm_mix, v_w_in, v_pool_grp, v_pool_scale, v_w_pool_proj, v_q_a_norm, v_w_q_up, v_kv_a_norm, v_w_kv_up, v_q_norm_nope, v_q_norm_rope, v_k_norm_nope, v_k_norm_rope, v_w_mla_proj, v_w_out, v_norm_ffn2, v_w_ffn2_in, v_w_ffn2_out):
    args = dict(locals())
    wts = {n: args[n][0] for n in WEIGHTS}
    mom = {n: args["m_" + n][0] for n in WEIGHTS}
    var = {n: args["v_" + n][0] for n in WEIGHTS}
    nb, seq, dm = x.shape
    tokens = nb * seq
    xi, yi, ci = _coords()
    chip = 2 * xi + yi
    dev = 4 * xi + 2 * yi + ci

    c_all = _gather8(c, name="gather_c").reshape(N_DEV * nb, dm)
    ncol = w_ada.shape[2]
    b_cols = lax.dynamic_slice_in_dim(wts["b_ada"].reshape(1, -1), chip * ncol, ncol, axis=1)
    mod_cols = _ada_fwd(c_all, wts["w_ada"], b_cols, name="ada_fwd")
    mod_all = _gather8(mod_cols, name="gather_mod")
    mine = lax.dynamic_slice_in_dim(mod_all, dev * nb, nb, axis=1)
    mod = jnp.concatenate([mine[0], mine[2], mine[4], mine[6]], axis=1)
    mod3 = mod.reshape(nb, 9, dm)

    cidx = ci.astype(jnp.int32).reshape(1)
    plan = _ExchangePlan({n: wts[n].astype(MX) for n in BIG}, cidx)
    plan.W["pool_grp"] = wts["pool_grp"].astype(MX)
    plan.landed("gather_ffn1", _comm_only(plan.rider("gather_ffn1"), name="gather_ffn1"))
    P = {"norm_ffn1": wts["norm_ffn1"].reshape(1, dm), "norm_mix": wts["norm_mix"].reshape(1, dm),
         "norm_ffn2": wts["norm_ffn2"].reshape(1, dm), "pool_scale": wts["pool_scale"].reshape(1, POOL_W),
         "q_a_norm": wts["q_a_norm"].reshape(1, QL), "kv_a_norm": wts["kv_a_norm"].reshape(1, KVL),
         "q_gain": _gain_slab(wts["q_norm_nope"].reshape(1, NOPE), wts["q_norm_rope"].reshape(1, ROPE)),
         "k_gain": _gain_slab(wts["k_norm_nope"].reshape(1, NOPE), wts["k_norm_rope"].reshape(1, ROPE))}
    cos, sin = _rope_tables(positions.reshape(tokens))

    loss8, grad_x, small, dmod = _layer_fwd_bwd(x.reshape(tokens, dm), loss_target.reshape(tokens, dm), mod3, cos, sin, plan, P)
    loss = lax.psum(loss8[0, 0], ("x", "y", "c"))


    qg, kg = small["q_gain"], small["k_gain"]
    per_seq = {"b_ada": dmod.reshape(nb, 9 * dm), "norm_ffn1": small["norm_ffn1"], "norm_mix": small["norm_mix"],
               "pool_scale": small["pool_scale"], "q_a_norm": small["q_a_norm"],
               "kv_a_norm": small["kv_a_norm"], "q_norm_nope": qg[:, :NOPE], "q_norm_rope": qg[:, NOPE:NOPE + ROPE],
               "k_norm_nope": kg[:, :NOPE], "k_norm_rope": kg[:, KR_LANE:KR_LANE + ROPE], "norm_ffn2": small["norm_ffn2"]}
    slabs, spans = [], None
    for s in range(nb):
        slab, spans = _pack_small({n: per_seq[n][s] for n in SMALL_SEQ}, SMALL_SEQ)
        slabs.append(slab)
    rows = slabs[0].shape[0]
    slabs.append(_to_rows(small["pool_grp"]))
    gathered_small = _run(plan, _gather8, jnp.concatenate(slabs, axis=0), name="gather_small")
    joined = _comm_only(_join_comm([plan.reduced[n] for n in BIG]), name="rs_join_halves")
    grads = {n: g for n, g in zip(BIG, joined)}
    small_sum = _sum_small(gathered_small, nb, rows, name="sum_small")
    small_grads = _unpack_small(small_sum[:rows], spans)
    small_grads["pool_grp"] = small_sum[rows:]
    for n in SMALL:
        grads[n] = small_grads[n].reshape(wts[n].shape)
    at, _, _ = spans["b_ada"]
    nrow_b = 9 * dm // SLAB_W
    dmod_all = gathered_small[:, :nb * rows].reshape(N_DEV * nb, rows, SLAB_W)[:, at:at + nrow_b].reshape(N_DEV * nb, 9 * dm)
    dmod_cols = lax.dynamic_slice_in_dim(dmod_all, chip * ncol, ncol, axis=1)
    grads["w_ada"] = _ada_bwd(c_all, dmod_cols, name="ada_bwd")

    delta, new_m, new_v = {}, {}, {}
    for n in ("w_ada",) + BIG:
        delta[n], new_m[n], new_v[n] = _adamw(wts[n], grads[n], mom[n], var[n], name="adamw_" + n)
    w_slab, sp = _pack_small({n: wts[n] for n in SMALL})
    g_slab, _ = _pack_small({n: grads[n] for n in SMALL})
    m_slab, _ = _pack_small({n: mom[n] for n in SMALL})
    v_slab, _ = _pack_small({n: var[n] for n in SMALL})
    d_s, m_s, v_s = _adamw(w_slab, g_slab, m_slab, v_slab, name="adamw_small")
    for dst, slab in ((delta, d_s), (new_m, m_s), (new_v, v_s)):
        un = _unpack_small(slab, sp)
        for n in SMALL:
            dst[n] = un[n]

    def lead(a, n):
        return a.reshape((1,) + wts[n].shape)

    return (loss, grad_x.reshape(nb, seq, dm), *[lead(grads[n], n) for n in WEIGHTS], *[lead(delta[n], n) for n in WEIGHTS],
            *[lead(new_m[n], n) for n in WEIGHTS], *[lead(new_v[n], n) for n in WEIGHTS])
```

```python
import functools
import math

import jax
import jax.numpy as jnp
from jax import lax
from jax.experimental import pallas as pl
from jax.experimental.pallas import tpu as pltpu

F32 = jnp.float32
MX = jnp.bfloat16

D = 1024
DFF = 2816
NH = 8
POOL_W = 512
POOL_G = 4
QL = 384
KVL = 256
ROPE = 32
NOPE = 64
LANE = 128
SUBLANE = 8
EPS = 1e-6
ATTN_SCALE = 1.0 / math.sqrt(96.0)
NEG = -1e30

Z_UP, Z_QL, Z_KV, Z_KR, Z_GP, Z_GM, Z_W = 0, 512, 896, 1152, 1280, 2304, 3328
KR_LANE = 64
LAT_W = 1280

ADAM_LR, ADAM_B1, ADAM_B2, ADAM_EPS, ADAM_WD, ADAM_STEP = 0.001, 0.9, 0.999, 1e-08, 0.01, 10

VMEM_LIMIT = 48 * 1024 * 1024
MESH = pl.DeviceIdType.MESH
N_DEV = 8
N_CHIP = 4


class _Comm:
    def __init__(self, ins, out_shapes, sems, start, finish, aliases=None):
        self.ins, self.out_shapes, self.sems = list(ins), list(out_shapes), list(sems)
        self.start, self.finish = start, finish
        self.aliases = aliases or {}


def _pcall(body, *, name, out_shape, grid=(), in_specs=None, out_specs=None, scratch=(), grid_spec=None, aliases=None,
           comm=None):
    params = pltpu.CompilerParams(vmem_limit_bytes=VMEM_LIMIT)
    kw = dict(name=name, compiler_params=params)
    if comm is None:
        if aliases:
            kw["input_output_aliases"] = aliases
        if grid_spec is not None:
            return pl.pallas_call(body, grid_spec=grid_spec, out_shape=out_shape, **kw)
        return pl.pallas_call(body, grid=grid, in_specs=in_specs, out_specs=out_specs, scratch_shapes=scratch,
                              out_shape=out_shape, **kw)
    single = not isinstance(out_shape, (list, tuple))
    outs = [out_shape] if single else list(out_shape)
    ospecs = [out_specs] if single else list(out_specs)
    n_in, n_out, n_ci, n_co, n_scr = len(in_specs), len(outs), len(comm.ins), len(comm.out_shapes), len(scratch)
    io = dict(aliases or {})
    io.update({n_in + i: n_out + o for i, o in comm.aliases.items()})

    def riding(*refs):
        ins, cins = refs[:n_in], refs[n_in:n_in + n_ci]
        at = n_in + n_ci
        os_, couts = refs[at:at + n_out], refs[at + n_out:at + n_out + n_co]
        at += n_out + n_co
        scr, csems = refs[at:at + n_scr], refs[at + n_scr:]
        if grid:
            first = functools.reduce(jnp.logical_and, [pl.program_id(d) == 0 for d in range(len(grid))])
            last = functools.reduce(jnp.logical_and, [pl.program_id(d) == grid[d] - 1 for d in range(len(grid))])
            pl.when(first)(lambda: comm.start(cins, couts, csems))
            body(*ins, *os_, *scr)
            pl.when(last)(lambda: comm.finish(cins, couts, csems))
        else:
            comm.start(cins, couts, csems)
            body(*ins, *os_, *scr)
            comm.finish(cins, couts, csems)

    call = pl.pallas_call(riding, grid=grid, in_specs=list(in_specs) + [HBM_SPEC] * n_ci, out_specs=ospecs + [HBM_SPEC] * n_co,
                          out_shape=outs + comm.out_shapes, scratch_shapes=list(scratch) + comm.sems,
                          input_output_aliases=io, **kw)

    def run(*args):
        res = call(*args, *comm.ins)
        main = list(res[:n_out])
        return (main[0] if single else main), list(res[n_out:])

    return run


def _pick(dim, target):
    best = None
    for t in range(LANE, min(dim, target) + 1, LANE):
        if dim % t == 0:
            best = t
    return dim if best is None else best


def _sds(shape, dtype):
    return jax.ShapeDtypeStruct(shape, dtype)


def _dw(a, g, *, name, tn=1024):
    return _mm(a, g, mode="tn", out_dtype=F32, name=name, tm=256, tn=tn, tk=a.shape[0], n_outer=True)


def _mm(a, b, *, mode, out_dtype, name, tm=1024, tn=1024, tk=4096, n_outer=False, comm=None):
    if mode == "nn":
        (M, K), (K2, N) = a.shape, b.shape
    elif mode == "nt":
        (M, K), (N, K2) = a.shape, b.shape
    else:
        (K, M), (K2, N) = a.shape, b.shape
    assert K == K2, (name, a.shape, b.shape)
    tm, tn, tk = _pick(M, tm), _pick(N, tn), _pick(K, tk)
    nk = K // tk
    if n_outer:
        ij = lambda g0, g1: (g1, g0)
        grid = (N // tn, M // tm, nk)
    else:
        ij = lambda g0, g1: (g0, g1)
        grid = (M // tm, N // tn, nk)
    if mode == "tn":
        a_spec = pl.BlockSpec((tk, tm), lambda g0, g1, k: (k, ij(g0, g1)[0]))
    else:
        a_spec = pl.BlockSpec((tm, tk), lambda g0, g1, k: (ij(g0, g1)[0], k))
    if mode == "nt":
        b_spec = pl.BlockSpec((tn, tk), lambda g0, g1, k: (ij(g0, g1)[1], k))
    else:
        b_spec = pl.BlockSpec((tk, tn), lambda g0, g1, k: (k, ij(g0, g1)[1]))
    o_spec = pl.BlockSpec((tm, tn), lambda g0, g1, k: ij(g0, g1))
    dn = {"nn": (((1,), (0,)), ((), ())), "nt": (((1,), (1,)), ((), ())), "tn": (((0,), (0,)), ((), ()))}[mode]

    def dot(a_ref, b_ref):
        return lax.dot_general(a_ref[...].astype(MX), b_ref[...].astype(MX), dn, preferred_element_type=F32)

    def body_one(a_ref, b_ref, o_ref):
        o_ref[...] = dot(a_ref, b_ref).astype(o_ref.dtype)

    def body_acc(a_ref, b_ref, o_ref, acc_ref):
        k = pl.program_id(2)
        part = dot(a_ref, b_ref)

        @pl.when(k == 0)
        def _():
            acc_ref[...] = part

        @pl.when(k > 0)
        def _():
            acc_ref[...] += part

        @pl.when(k == nk - 1)
        def _():
            o_ref[...] = acc_ref[...].astype(o_ref.dtype)

    return _pcall(body_one if nk == 1 else body_acc, name=name, out_shape=_sds((M, N), out_dtype), grid=grid,
                  in_specs=[a_spec, b_spec], out_specs=o_spec, scratch=[] if nk == 1 else [pltpu.VMEM((tm, tn), F32)],
                  comm=comm)(a, b)


def _gu_shard(q):
    return (q % 2) * 2 + q // 2


def _ffn_up(h, w_st, *, name, tm=512, comm=None):
    T, dm = h.shape
    hw = w_st.shape[2]
    tm = _pick(T, tm)

    def body(h_ref, wg_ref, wu_ref, gu_ref, a_ref):
        hv = h_ref[...]
        g = jnp.dot(hv, wg_ref[0], preferred_element_type=F32)
        u = jnp.dot(hv, wu_ref[0], preferred_element_type=F32)
        gu_ref[:, :hw] = g.astype(gu_ref.dtype)
        gu_ref[:, hw:] = u.astype(gu_ref.dtype)
        a_ref[...] = (g * _sigmoid(g) * u).astype(a_ref.dtype)

    return _pcall(body, name=name, grid=(T // tm, 2),
                  in_specs=[pl.BlockSpec((tm, dm), lambda i, j: (i, 0)), pl.BlockSpec((1, dm, hw), lambda i, j: (j, 0, 0)),
                            pl.BlockSpec((1, dm, hw), lambda i, j: (2 + j, 0, 0))],
                  out_specs=[pl.BlockSpec((tm, 2 * hw), lambda i, j: (i, j)), pl.BlockSpec((tm, hw), lambda i, j: (i, j))],
                  out_shape=[_sds((T, 4 * hw), MX), _sds((T, 2 * hw), MX)], comm=comm)(h, w_st, w_st)


def _ffn_dact(df, gu, w_out, *, name, tm=512, comm=None):
    T, dm = df.shape
    hw = gu.shape[1] // 4
    tm = _pick(T, tm)

    def body(df_ref, gu_ref, wo_ref, dgu_ref):
        da = lax.dot_general(df_ref[...], wo_ref[...], (((1,), (1,)), ((), ())), preferred_element_type=F32)
        g = gu_ref[:, :hw].astype(F32)
        u = gu_ref[:, hw:].astype(F32)
        s = _sigmoid(g)
        dgu_ref[:, :hw] = (da * u * (s * (1.0 + g * (1.0 - s)))).astype(dgu_ref.dtype)
        dgu_ref[:, hw:] = (da * (g * s)).astype(dgu_ref.dtype)

    return _pcall(body, name=name, grid=(T // tm, 2),
                  in_specs=[pl.BlockSpec((tm, dm), lambda i, j: (i, 0)), pl.BlockSpec((tm, 2 * hw), lambda i, j: (i, j)),
                            pl.BlockSpec((hw, dm), lambda i, j: (j, 0))],
                  out_specs=pl.BlockSpec((tm, 2 * hw), lambda i, j: (i, j)), out_shape=_sds(gu.shape, MX),
                  comm=comm)(df, gu, w_out)


def _ffn_dh(dgu, w_st, *, name, tm=1024, comm=None):
    T = dgu.shape[0]
    _, dm, hw = w_st.shape
    tm = _pick(T, tm)

    def body(d_ref, w_ref, o_ref, acc_ref):
        q = pl.program_id(1)
        part = lax.dot_general(d_ref[...], w_ref[0], (((1,), (1,)), ((), ())), preferred_element_type=F32)

        @pl.when(q == 0)
        def _():
            acc_ref[...] = part

        @pl.when(q > 0)
        def _():
            acc_ref[...] += part

        @pl.when(q == 3)
        def _():
            o_ref[...] = acc_ref[...]

    return _pcall(body, name=name, grid=(T // tm, 4),
                  in_specs=[pl.BlockSpec((tm, hw), lambda i, q: (i, q)), pl.BlockSpec((1, dm, hw), lambda i, q: (_gu_shard(q), 0, 0))],
                  out_specs=pl.BlockSpec((tm, dm), lambda i, q: (i, 0)), out_shape=_sds((T, dm), F32),
                  scratch=[pltpu.VMEM((tm, dm), F32)], comm=comm)(dgu, w_st)


def _ffn_dw_in(h, dgu, *, name, tm=256, comm=None):
    T, dm = h.shape
    hw = dgu.shape[1] // 4
    tm = _pick(dm, tm)

    def body(h_ref, d_ref, o_ref):
        o_ref[0] = lax.dot_general(h_ref[...], d_ref[...], (((0,), (0,)), ((), ())), preferred_element_type=F32)

    return _pcall(body, name=name, grid=(4, dm // tm),
                  in_specs=[pl.BlockSpec((T, tm), lambda q, i: (0, i)), pl.BlockSpec((T, hw), lambda q, i: (0, q))],
                  out_specs=pl.BlockSpec((1, tm, hw), lambda q, i: (_gu_shard(q), i, 0)),
                  out_shape=_sds((4, dm, hw), F32), comm=comm)(h, dgu)


def _rsq(x):
    return lax.rsqrt(jnp.mean(x * x, axis=-1, keepdims=True) + EPS)


def _sigmoid(x):
    return 1.0 / (1.0 + jnp.exp(-x))


def _row_spec(tm, w):
    return pl.BlockSpec((tm, w), lambda i: (i, 0))


def _fwd_block(x_prev, f_prev, mod3, gain, *, sub, coef, name, tm=256):
    T, dm = x_prev.shape
    tps = (T // mod3.shape[0]) // tm
    has_f = f_prev is not None
    mod_spec = pl.BlockSpec((1, 9, dm), lambda i: (i // tps, 0, 0))
    vec_spec = pl.BlockSpec((1, dm), lambda i: (0, 0))

    def body(*refs):
        if has_f:
            x_ref, f_ref, mod_ref, n_ref, xo_ref, h_ref = refs
            x = x_ref[...] + coef * mod_ref[0, 3 * sub - 1:3 * sub, :] * f_ref[...]
            xo_ref[...] = x
        else:
            x_ref, mod_ref, n_ref, h_ref = refs
            x = x_ref[...]
        xn = x * _rsq(x) * n_ref[...]
        h = xn * (1.0 + mod_ref[0, 3 * sub + 1:3 * sub + 2, :]) + mod_ref[0, 3 * sub:3 * sub + 1, :]
        h_ref[...] = h.astype(h_ref.dtype)

    row = _row_spec(tm, dm)
    if has_f:
        return _pcall(body, name=name, grid=(T // tm,), in_specs=[row, row, mod_spec, vec_spec], out_specs=[row, row],
                      out_shape=[_sds((T, dm), F32), _sds((T, dm), MX)])(x_prev, f_prev, mod3, gain)
    return _pcall(body, name=name, grid=(T // tm,), in_specs=[row, mod_spec, vec_spec], out_specs=row,
                  out_shape=_sds((T, dm), MX))(x_prev, mod3, gain)


def _final(x2, f2, tgt, mod3, *, name, tm=256):
    T, dm = x2.shape
    tps = (T // mod3.shape[0]) // tm
    mod_spec = pl.BlockSpec((1, 9, dm), lambda i: (i // tps, 0, 0))
    row = _row_spec(tm, dm)
    stat_spec = pl.BlockSpec((1, SUBLANE, dm), lambda i: (i // tps, 0, 0))
    loss_spec = pl.BlockSpec((SUBLANE, LANE), lambda i: (0, 0))

    def body(x_ref, f_ref, t_ref, mod_ref, dy_ref, df_ref, st_ref, loss_ref):
        i = pl.program_id(0)
        g = mod_ref[0, 8:9, :]
        f = f_ref[...]
        err = x_ref[...] + 0.5 * g * f - t_ref[...]
        dy = err * (1.0 / dm)
        dy_ref[...] = dy
        df_ref[...] = (0.5 * g * dy).astype(df_ref.dtype)
        dgate = jnp.sum(0.5 * dy * f, axis=0, keepdims=True)
        part = 0.5 * jnp.sum(jnp.sum(err * err, axis=0, keepdims=True), axis=1, keepdims=True) * (1.0 / dm)

        @pl.when(i % tps == 0)
        def _():
            st_ref[...] = jnp.zeros_like(st_ref)

        @pl.when(i == 0)
        def _():
            loss_ref[...] = jnp.zeros_like(loss_ref)

        st_ref[0, 0:1, :] += dgate
        loss_ref[...] += jnp.broadcast_to(part, loss_ref.shape)

    return _pcall(body, name=name, grid=(T // tm,), in_specs=[row, row, row, mod_spec],
                  out_specs=[row, row, stat_spec, loss_spec],
                  out_shape=[_sds((T, dm), F32), _sds((T, dm), MX), _sds((mod3.shape[0], SUBLANE, dm), F32),
                             _sds((SUBLANE, LANE), F32)])(x2, f2, tgt, mod3)


def _bwd_block(x_cur, dh, dx_in, f_prev, mod3, gain, *, sub, coef, name, tm=256, comm=None):
    T, dm = x_cur.shape
    nb = mod3.shape[0]
    tps = (T // nb) // tm
    has_f = f_prev is not None
    mod_spec = pl.BlockSpec((1, 9, dm), lambda i: (i // tps, 0, 0))
    vec_spec = pl.BlockSpec((1, dm), lambda i: (0, 0))
    stat_spec = pl.BlockSpec((1, SUBLANE, dm), lambda i: (i // tps, 0, 0))
    row = _row_spec(tm, dm)

    def body(*refs):
        if has_f:
            x_ref, dh_ref, dxi_ref, f_ref, mod_ref, n_ref, dx_ref, df_ref, st_ref = refs
        else:
            x_ref, dh_ref, dxi_ref, mod_ref, n_ref, dx_ref, st_ref = refs
        i = pl.program_id(0)
        x = x_ref[...]
        r = _rsq(x)
        xhat = x * r
        n = n_ref[...]
        dhv = dh_ref[...]
        d_shift = jnp.sum(dhv, axis=0, keepdims=True)
        d_scale = jnp.sum(dhv * (xhat * n), axis=0, keepdims=True)
        dxn = dhv * (1.0 + mod_ref[0, 3 * sub + 1:3 * sub + 2, :])
        d_gain = jnp.sum(dxn * xhat, axis=0, keepdims=True)
        dxhat = dxn * n
        dx = dxi_ref[...] + r * (dxhat - xhat * jnp.mean(dxhat * xhat, axis=-1, keepdims=True))
        dx_ref[...] = dx

        @pl.when(i % tps == 0)
        def _():
            st_ref[...] = jnp.zeros_like(st_ref)

        st_ref[0, 0:1, :] += d_shift
        st_ref[0, 1:2, :] += d_scale
        st_ref[0, 2:3, :] += d_gain
        if has_f:
            f = f_ref[...]
            st_ref[0, 3:4, :] += jnp.sum(coef * dx * f, axis=0, keepdims=True)
            df_ref[...] = (coef * mod_ref[0, 3 * sub - 1:3 * sub, :] * dx).astype(df_ref.dtype)

    st_shape = _sds((nb, SUBLANE, dm), F32)
    if has_f:
        return _pcall(body, name=name, grid=(T // tm,), in_specs=[row, row, row, row, mod_spec, vec_spec],
                      out_specs=[row, row, stat_spec],
                      out_shape=[_sds((T, dm), F32), _sds((T, dm), MX), st_shape], comm=comm)(x_cur, dh, dx_in, f_prev, mod3, gain)
    return _pcall(body, name=name, grid=(T // tm,), in_specs=[row, row, row, mod_spec, vec_spec],
                  out_specs=[row, stat_spec], out_shape=[_sds((T, dm), F32), st_shape], comm=comm)(x_cur, dh, dx_in, mod3, gain)


def _merge_fwd(z, br_pool, br_mla, *, name, tm=256):
    T = z.shape[0]

    def body(z_ref, bp_ref, bm_ref, o_ref):
        gp = z_ref[:, Z_GP:Z_GP + D]
        gm = z_ref[:, Z_GM:Z_GM + D]
        o_ref[...] = (_sigmoid(gp) * bp_ref[...] + _sigmoid(gm) * bm_ref[...]).astype(o_ref.dtype)

    return _pcall(body, name=name, grid=(T // tm,), in_specs=[_row_spec(tm, Z_W), _row_spec(tm, D), _row_spec(tm, D)],
                  out_specs=_row_spec(tm, D), out_shape=_sds((T, D), MX))(z, br_pool, br_mla)


def _merge_bwd(z, br_pool, br_mla, dmerged, *, name, tm=256):
    T = z.shape[0]

    def body(z_ref, bp_ref, bm_ref, dm_ref, dbp_ref, dbm_ref, dg_ref):
        dm = dm_ref[...]
        sp = _sigmoid(z_ref[:, Z_GP:Z_GP + D])
        sm = _sigmoid(z_ref[:, Z_GM:Z_GM + D])
        dbp_ref[...] = (dm * sp).astype(dbp_ref.dtype)
        dbm_ref[...] = (dm * sm).astype(dbm_ref.dtype)
        dg_ref[:, :D] = (dm * bp_ref[...] * sp * (1.0 - sp)).astype(dg_ref.dtype)
        dg_ref[:, D:] = (dm * bm_ref[...] * sm * (1.0 - sm)).astype(dg_ref.dtype)

    return _pcall(body, name=name, grid=(T // tm,),
                  in_specs=[_row_spec(tm, Z_W), _row_spec(tm, D), _row_spec(tm, D), _row_spec(tm, D)],
                  out_specs=[_row_spec(tm, D), _row_spec(tm, D), _row_spec(tm, 2 * D)],
                  out_shape=[_sds((T, D), MX), _sds((T, D), MX), _sds((T, 2 * D), MX)])(z, br_pool, br_mla, dmerged)


def _shift_down(x, k, row):
    return jnp.where(row >= k, pltpu.roll(x, k, 0), 0.0)


def _shift_up(x, k, row, n):
    return jnp.where(row < n - k, pltpu.roll(x, n - k, 0), 0.0)


def _pool_fwd(z, pool_grp, pool_scale, *, nb, name):
    T = z.shape[0]
    S = T // nb
    blk = pl.BlockSpec((S, LANE), lambda b, g: (b, g))

    def body(u_ref, w_ref, s_ref, pooled_ref, mixed_ref, scaled_ref):
        g = pl.program_id(1)
        u = u_ref[...]
        row = lax.broadcasted_iota(jnp.int32, u.shape, 0)
        s2 = u + _shift_down(u, 1, row)
        s4 = s2 + _shift_down(s2, 2, row)
        s8 = s4 + _shift_down(s4, 4, row)
        s16 = s8 + _shift_down(s8, 8, row)
        win = jnp.where(g == 0, s2, jnp.where(g == 1, s4, jnp.where(g == 2, s8, s16)))
        width = lax.shift_left(jnp.int32(2), g)
        cnt = jnp.minimum(row + 1, width).astype(F32)
        pooled = (win / cnt - u).astype(MX)
        pooled_ref[...] = pooled
        mixed = jnp.dot(pooled, w_ref[0], preferred_element_type=F32)
        mixed_ref[...] = mixed
        scaled_ref[...] = (mixed * s_ref[...]).astype(scaled_ref.dtype)

    return _pcall(body, name=name, grid=(nb, POOL_G),
                  in_specs=[blk, pl.BlockSpec((1, LANE, LANE), lambda b, g: (g, 0, 0)),
                            pl.BlockSpec((1, LANE), lambda b, g: (0, g))],
                  out_specs=[blk, blk, blk],
                  out_shape=[_sds((T, POOL_W), MX), _sds((T, POOL_W), F32), _sds((T, POOL_W), MX)])(z, pool_grp, pool_scale)


def _pool_bwd(dscaled, mixed, pooled, pool_grp, pool_scale, *, nb, name):
    T = dscaled.shape[0]
    S = T // nb
    blk = pl.BlockSpec((S, LANE), lambda g, b: (b, g))

    def body(ds_ref, mixed_ref, pooled_ref, w_ref, s_ref, du_ref, dw_ref, dsc_ref):
        g = pl.program_id(0)
        b = pl.program_id(1)
        ds = ds_ref[...]
        dsc_ref[0] = jnp.sum(ds * mixed_ref[...], axis=0, keepdims=True)
        dmixed = (ds * s_ref[...]).astype(MX)
        dw = lax.dot_general(pooled_ref[...], dmixed, (((0,), (0,)), ((), ())), preferred_element_type=F32)

        @pl.when(b == 0)
        def _():
            dw_ref[0] = dw

        @pl.when(b > 0)
        def _():
            dw_ref[0] += dw
        dpooled = lax.dot_general(dmixed, w_ref[0], (((1,), (1,)), ((), ())), preferred_element_type=F32)
        row = lax.broadcasted_iota(jnp.int32, dpooled.shape, 0)
        width = lax.shift_left(jnp.int32(2), g)
        q = dpooled / jnp.minimum(row + 1, width).astype(F32)
        r2 = q + _shift_up(q, 1, row, S)
        r4 = r2 + _shift_up(r2, 2, row, S)
        r8 = r4 + _shift_up(r4, 4, row, S)
        r16 = r8 + _shift_up(r8, 8, row, S)
        win = jnp.where(g == 0, r2, jnp.where(g == 1, r4, jnp.where(g == 2, r8, r16)))
        du_ref[...] = (win - dpooled).astype(du_ref.dtype)

    return _pcall(body, name=name, grid=(POOL_G, nb),
                  in_specs=[blk, blk, blk, pl.BlockSpec((1, LANE, LANE), lambda g, b: (g, 0, 0)),
                            pl.BlockSpec((1, LANE), lambda g, b: (0, g))],
                  out_specs=[blk, pl.BlockSpec((1, LANE, LANE), lambda g, b: (g, 0, 0)),
                             pl.BlockSpec((1, 1, LANE), lambda g, b: (b, 0, g))],
                  out_shape=[_sds((T, POOL_W), MX), _sds((POOL_G, LANE, LANE), F32), _sds((nb, 1, POOL_W), F32)],
                  )(dscaled, mixed, pooled, pool_grp, pool_scale)


def _lat_fwd(z, q_gain, kv_gain, *, name, tm=256):
    T = z.shape[0]

    def body(z_ref, qg_ref, kg_ref, qn_ref, kvn_ref):
        ql = z_ref[:, Z_QL:Z_QL + QL]
        kv = z_ref[:, Z_KV:Z_KV + KVL]
        qn_ref[...] = (ql * _rsq(ql) * qg_ref[...]).astype(qn_ref.dtype)
        kvn_ref[...] = (kv * _rsq(kv) * kg_ref[...]).astype(kvn_ref.dtype)

    return _pcall(body, name=name, grid=(T // tm,),
                  in_specs=[_row_spec(tm, LAT_W), pl.BlockSpec((1, QL), lambda i: (0, 0)), pl.BlockSpec((1, KVL), lambda i: (0, 0))],
                  out_specs=[_row_spec(tm, QL), _row_spec(tm, KVL)],
                  out_shape=[_sds((T, QL), MX), _sds((T, KVL), MX)])(z, q_gain, kv_gain)


def _lat_bwd(z, dqn, dkvn, q_gain, kv_gain, *, nb, name, tm=256):
    T = z.shape[0]
    tps = (T // nb) // tm

    def norm_bwd(x, dy, gain):
        r = _rsq(x)
        xhat = x * r
        dgain = jnp.sum(dy * xhat, axis=0, keepdims=True)
        dxhat = dy * gain
        return r * (dxhat - xhat * jnp.mean(dxhat * xhat, axis=-1, keepdims=True)), dgain

    def body(z_ref, dq_ref, dkv_ref, qg_ref, kg_ref, dql_ref, dkvl_ref, sq_ref, sk_ref):
        i = pl.program_id(0)
        dql, dqg = norm_bwd(z_ref[:, Z_QL:Z_QL + QL], dq_ref[...], qg_ref[...])
        dkvl, dkg = norm_bwd(z_ref[:, Z_KV:Z_KV + KVL], dkv_ref[...], kg_ref[...])
        dql_ref[...] = dql.astype(dql_ref.dtype)
        dkvl_ref[...] = dkvl.astype(dkvl_ref.dtype)

        @pl.when(i % tps == 0)
        def _():
            sq_ref[...] = jnp.zeros_like(sq_ref)
            sk_ref[...] = jnp.zeros_like(sk_ref)

        sq_ref[0, 0:1, :] += dqg
        sk_ref[0, 0:1, :] += dkg

    return _pcall(body, name=name, grid=(T // tm,),
                  in_specs=[_row_spec(tm, LAT_W), _row_spec(tm, QL), _row_spec(tm, KVL),
                            pl.BlockSpec((1, QL), lambda i: (0, 0)), pl.BlockSpec((1, KVL), lambda i: (0, 0))],
                  out_specs=[_row_spec(tm, QL), _row_spec(tm, KVL),
                             pl.BlockSpec((1, SUBLANE, QL), lambda i: (i // tps, 0, 0)),
                             pl.BlockSpec((1, SUBLANE, KVL), lambda i: (i // tps, 0, 0))],
                  out_shape=[_sds((T, QL), MX), _sds((T, KVL), MX), _sds((nb, SUBLANE, QL), F32), _sds((nb, SUBLANE, KVL), F32)],
                  )(z, dqn, dkvn, q_gain, kv_gain)


def _lane_masks(shape):
    lane = lax.broadcasted_iota(jnp.int32, shape, len(shape) - 1)
    m_n = lane < NOPE
    m_r = jnp.logical_and(lane >= KR_LANE, lane < KR_LANE + ROPE)
    first_half = lane < KR_LANE + ROPE // 2
    return m_n, m_r, first_half


def _rot(y, first_half):
    return jnp.where(first_half, -pltpu.roll(y, LANE - ROPE // 2, 1), pltpu.roll(y, ROPE // 2, 1))


def _rot_t(v, first_half, m_r):
    return jnp.where(m_r, jnp.where(first_half, pltpu.roll(v, LANE - ROPE // 2, 1), -pltpu.roll(v, ROPE // 2, 1)), 0.0)


def _prep_fwd(qp, kvp, z, cos, sin, q_gain, k_gain, *, name, tm=256):
    T = qp.shape[0]
    slab = pl.BlockSpec((tm, LANE), lambda i: (i, 0))
    kr_spec = pl.BlockSpec((tm, LANE), lambda i: (i, Z_KR // LANE))
    vec = pl.BlockSpec((1, LANE), lambda i: (0, 0))

    def body(qp_ref, kvp_ref, kr_ref, cos_ref, sin_ref, qg_ref, kg_ref, q_ref, k_ref, v_ref):
        m_n, m_r, first_half = _lane_masks((tm, LANE))
        c = cos_ref[...]
        s = sin_ref[...]
        qg = qg_ref[...]
        kg = kg_ref[...]
        xr = kr_ref[...]
        rr = lax.rsqrt(jnp.sum(xr * xr, axis=-1, keepdims=True) * (1.0 / ROPE) + EPS)
        yr = xr * rr * kg
        kr = jnp.where(m_r, yr * c + _rot(yr, first_half) * s, 0.0)
        for h in range(NH):
            x = qp_ref[:, h * LANE:(h + 1) * LANE]
            x2 = x * x
            rn = lax.rsqrt(jnp.sum(jnp.where(m_n, x2, 0.0), axis=-1, keepdims=True) * (1.0 / NOPE) + EPS)
            rq = lax.rsqrt(jnp.sum(jnp.where(m_r, x2, 0.0), axis=-1, keepdims=True) * (1.0 / ROPE) + EPS)
            y = x * jnp.where(m_n, rn, jnp.where(m_r, rq, 0.0)) * qg
            q_ref[:, h * LANE:(h + 1) * LANE] = (y * c + _rot(y, first_half) * s).astype(q_ref.dtype)
            xk = kvp_ref[:, h * LANE:(h + 1) * LANE]
            rk = lax.rsqrt(jnp.sum(jnp.where(m_n, xk * xk, 0.0), axis=-1, keepdims=True) * (1.0 / NOPE) + EPS)
            k_ref[:, h * LANE:(h + 1) * LANE] = (jnp.where(m_n, xk * rk * kg, 0.0) + kr).astype(k_ref.dtype)
        v_ref[...] = kvp_ref[:, NH * LANE:].astype(v_ref.dtype)

    return _pcall(body, name=name, grid=(T // tm,),
                  in_specs=[_row_spec(tm, NH * LANE), _row_spec(tm, NH * LANE + NH * NOPE), kr_spec, slab, slab, vec, vec],
                  out_specs=[_row_spec(tm, NH * LANE), _row_spec(tm, NH * LANE), _row_spec(tm, NH * NOPE)],
                  out_shape=[_sds((T, NH * LANE), MX), _sds((T, NH * LANE), MX), _sds((T, NH * NOPE), MX)],
                  )(qp, kvp, z, cos, sin, q_gain, k_gain)


def _prep_bwd(dq, dk, dv, qp, kvp, z, cos, sin, q_gain, k_gain, *, nb, name, tm=256):
    T = qp.shape[0]
    tps = (T // nb) // tm
    slab = pl.BlockSpec((tm, LANE), lambda i: (i, 0))
    kr_spec = pl.BlockSpec((tm, LANE), lambda i: (i, Z_KR // LANE))
    vec = pl.BlockSpec((1, LANE), lambda i: (0, 0))

    def body(dq_ref, dk_ref, dv_ref, qp_ref, kvp_ref, kr_ref, cos_ref, sin_ref, qg_ref, kg_ref,
             dqp_ref, dkvp_ref, dkr_ref, st_ref):
        i = pl.program_id(0)
        m_n, m_r, first_half = _lane_masks((tm, LANE))
        c = cos_ref[...]
        s = sin_ref[...]
        qg = qg_ref[...]
        kg = kg_ref[...]
        dqg = jnp.zeros((1, LANE), F32)
        dkg = jnp.zeros((1, LANE), F32)
        dkr_sum = jnp.zeros((tm, LANE), F32)
        for h in range(NH):
            x = qp_ref[:, h * LANE:(h + 1) * LANE]
            x2 = x * x
            rn = lax.rsqrt(jnp.sum(jnp.where(m_n, x2, 0.0), axis=-1, keepdims=True) * (1.0 / NOPE) + EPS)
            rq = lax.rsqrt(jnp.sum(jnp.where(m_r, x2, 0.0), axis=-1, keepdims=True) * (1.0 / ROPE) + EPS)
            rfac = jnp.where(m_n, rn, jnp.where(m_r, rq, 0.0))
            xhat = x * rfac
            do = dq_ref[:, h * LANE:(h + 1) * LANE]
            dy = do * c + _rot_t(do * s, first_half, m_r)
            dqg = dqg + jnp.sum(dy * xhat, axis=0, keepdims=True)
            dxhat = dy * qg
            t = dxhat * xhat
            mean_n = jnp.sum(jnp.where(m_n, t, 0.0), axis=-1, keepdims=True) * (1.0 / NOPE)
            mean_r = jnp.sum(jnp.where(m_r, t, 0.0), axis=-1, keepdims=True) * (1.0 / ROPE)
            dqp_ref[:, h * LANE:(h + 1) * LANE] = (
                rfac * (dxhat - xhat * jnp.where(m_n, mean_n, jnp.where(m_r, mean_r, 0.0)))).astype(dqp_ref.dtype)

            xk = kvp_ref[:, h * LANE:(h + 1) * LANE]
            rk = lax.rsqrt(jnp.sum(jnp.where(m_n, xk * xk, 0.0), axis=-1, keepdims=True) * (1.0 / NOPE) + EPS)
            khat = jnp.where(m_n, xk * rk, 0.0)
            dko = dk_ref[:, h * LANE:(h + 1) * LANE]
            dkn = jnp.where(m_n, dko, 0.0)
            dkg = dkg + jnp.sum(dkn * khat, axis=0, keepdims=True)
            dkhat = dkn * kg
            mean_k = jnp.sum(dkhat * khat, axis=-1, keepdims=True) * (1.0 / NOPE)
            dkvp_ref[:, h * LANE:(h + 1) * LANE] = jnp.where(m_n, rk * (dkhat - khat * mean_k), 0.0).astype(dkvp_ref.dtype)
            dkr_sum = dkr_sum + jnp.where(m_r, dko, 0.0)
        dkvp_ref[:, NH * LANE:] = dv_ref[...].astype(dkvp_ref.dtype)

        xr = kr_ref[...]
        rr = lax.rsqrt(jnp.sum(xr * xr, axis=-1, keepdims=True) * (1.0 / ROPE) + EPS)
        rhat = xr * rr
        dyr = dkr_sum * c + _rot_t(dkr_sum * s, first_half, m_r)
        dkg = dkg + jnp.sum(dyr * rhat, axis=0, keepdims=True)
        drhat = dyr * kg
        mean_kr = jnp.sum(drhat * rhat, axis=-1, keepdims=True) * (1.0 / ROPE)
        dkr_ref[...] = jnp.where(m_r, rr * (drhat - rhat * mean_kr), 0.0).astype(dkr_ref.dtype)

        @pl.when(i % tps == 0)
        def _():
            st_ref[...] = jnp.zeros_like(st_ref)

        st_ref[0, 0:1, :] += dqg
        st_ref[0, 1:2, :] += dkg

    return _pcall(body, name=name, grid=(T // tm,),
                  in_specs=[_row_spec(tm, NH * LANE), _row_spec(tm, NH * LANE), _row_spec(tm, NH * NOPE),
                            _row_spec(tm, NH * LANE), _row_spec(tm, NH * LANE + NH * NOPE), kr_spec, slab, slab, vec, vec],
                  out_specs=[_row_spec(tm, NH * LANE), _row_spec(tm, NH * LANE + NH * NOPE), slab,
                             pl.BlockSpec((1, SUBLANE, LANE), lambda i: (i // tps, 0, 0))],
                  out_shape=[_sds((T, NH * LANE), MX), _sds((T, NH * LANE + NH * NOPE), MX), _sds((T, LANE), MX),
                             _sds((nb, SUBLANE, LANE), F32)],
                  )(dq, dk, dv, qp, kvp, z, cos, sin, q_gain, k_gain)


def _lower_triangle(t):
    return lax.broadcasted_iota(jnp.int32, (t, t), 1) <= lax.broadcasted_iota(jnp.int32, (t, t), 0)


def _attn_fwd(q, k, v, *, nb, name, tq=512, comm=None):
    T = q.shape[0]
    S = T // nb
    tq = _pick(S, tq)
    nq = S // tq
    tk = tq
    npair = NH // 2

    def body(q_ref, k_ref, v_ref, o_ref, lse_ref):
        qi = pl.program_id(2)
        lane = lax.broadcasted_iota(jnp.int32, (tq, LANE), 1)
        qs = [q_ref[:, hh * LANE:(hh + 1) * LANE] for hh in range(2)]

        def block(j, carry, diagonal):
            k0 = pl.multiple_of(j * tk, tk)
            vb = v_ref[pl.ds(k0, tk), :]
            new = []
            for hh in range(2):
                m, l, acc = carry[hh]
                kb = k_ref[pl.ds(k0, tk), hh * LANE:(hh + 1) * LANE]
                s = lax.dot_general(qs[hh], kb, (((1,), (1,)), ((), ())), preferred_element_type=F32) * ATTN_SCALE
                if diagonal:
                    s = jnp.where(_lower_triangle(tq), s, NEG)
                m_new = jnp.maximum(m, jnp.max(s, axis=-1, keepdims=True))
                p = jnp.exp(s - m_new)
                alpha = jnp.exp(m - m_new)
                l = alpha * l + jnp.sum(p, axis=-1, keepdims=True)
                acc = alpha * acc + jnp.dot(p.astype(MX), vb, preferred_element_type=F32)
                new.append((m_new, l, acc))
            return tuple(new)

        init = tuple((jnp.full((tq, 1), NEG, F32), jnp.zeros((tq, 1), F32), jnp.zeros((tq, LANE), F32)) for _ in range(2))
        carry = lax.fori_loop(0, qi, lambda j, c: block(j, c, False), init)
        (m0, l0, acc0), (m1, l1, acc1) = block(qi, carry, True)
        o_ref[...] = jnp.where(lane < NOPE, acc0 / l0, acc1 / l1).astype(o_ref.dtype)
        lse_ref[...] = jnp.where(lane < NOPE, m0 + jnp.log(l0), m1 + jnp.log(l1))

    return _pcall(body, name=name, grid=(nb, npair, nq),
                  in_specs=[pl.BlockSpec((tq, 2 * LANE), lambda b, p, i: (b * nq + i, p)),
                            pl.BlockSpec((S, 2 * LANE), lambda b, p, i: (b, p)),
                            pl.BlockSpec((S, LANE), lambda b, p, i: (b, p))],
                  out_specs=[pl.BlockSpec((tq, LANE), lambda b, p, i: (b * nq + i, p)),
                             pl.BlockSpec((tq, LANE), lambda b, p, i: (b * nq + i, p))],
                  out_shape=[_sds((T, NH * NOPE), MX), _sds((T, NH * NOPE), F32)], comm=comm)(q, k, v)


def _attn_bwd(q, k, v, o, lse, do, *, nb, name, tq=512, comm=None):
    T = q.shape[0]
    S = T // nb
    tq = _pick(S, tq)
    nq = S // tq
    tk = tq
    npair = NH // 2

    def body(q_ref, k_ref, v_ref, o_ref, lse_ref, do_ref, dq_ref, dk_ref, dv_ref, delta_ref):
        lane = lax.broadcasted_iota(jnp.int32, (tq, LANE), 1)
        first = lane < NOPE
        dq_ref[...] = jnp.zeros_like(dq_ref)

        def delta_step(qi, _):
            q0 = pl.multiple_of(qi * tq, tq)
            prod = do_ref[pl.ds(q0, tq), :] * o_ref[pl.ds(q0, tq), :].astype(F32)
            d0 = jnp.sum(jnp.where(first, prod, 0.0), axis=-1, keepdims=True)
            d1 = jnp.sum(jnp.where(first, 0.0, prod), axis=-1, keepdims=True)
            delta_ref[pl.ds(q0, tq), :] = jnp.where(first, d0, d1)
            return 0

        lax.fori_loop(0, nq, delta_step, 0)

        def kv_step(kj, _):
            k0 = pl.multiple_of(kj * tk, tk)
            kbs = [k_ref[pl.ds(k0, tk), hh * LANE:(hh + 1) * LANE] for hh in range(2)]
            vb = v_ref[pl.ds(k0, tk), :]

            def q_block(qi, carry, diagonal):
                dk0, dk1, dv = carry
                dks = [dk0, dk1]
                q0 = pl.multiple_of(qi * tq, tq)
                dov = do_ref[pl.ds(q0, tq), :]
                lse_v = lse_ref[pl.ds(q0, tq), :]
                delta_v = delta_ref[pl.ds(q0, tq), :]
                for hh in range(2):
                    qh = q_ref[pl.ds(q0, tq), hh * LANE:(hh + 1) * LANE]
                    dob = jnp.where(first if hh == 0 else jnp.logical_not(first), dov, 0.0).astype(MX)
                    s = lax.dot_general(qh, kbs[hh], (((1,), (1,)), ((), ())), preferred_element_type=F32) * ATTN_SCALE
                    p = jnp.exp(s - lse_v[:, hh * NOPE:hh * NOPE + 1])
                    if diagonal:
                        p = jnp.where(_lower_triangle(tq), p, 0.0)
                    dp = lax.dot_general(dob, vb, (((1,), (1,)), ((), ())), preferred_element_type=F32)
                    ds = (p * (dp - delta_v[:, hh * NOPE:hh * NOPE + 1]) * ATTN_SCALE).astype(MX)
                    dks[hh] = dks[hh] + lax.dot_general(ds, qh, (((0,), (0,)), ((), ())), preferred_element_type=F32)
                    dv = dv + lax.dot_general(p.astype(MX), dob, (((0,), (0,)), ((), ())), preferred_element_type=F32)
                    dq_ref[pl.ds(q0, tq), hh * LANE:(hh + 1) * LANE] += jnp.dot(ds, kbs[hh], preferred_element_type=F32)
                return dks[0], dks[1], dv

            zero = jnp.zeros((tk, LANE), F32)
            carry = q_block(kj, (zero, zero, zero), True)
            dk0, dk1, dv = lax.fori_loop(kj + 1, nq, lambda qi, c: q_block(qi, c, False), carry)
            dk_ref[pl.ds(k0, tk), 0:LANE] = dk0
            dk_ref[pl.ds(k0, tk), LANE:2 * LANE] = dk1
            dv_ref[pl.ds(k0, tk), :] = dv
            return 0

        lax.fori_loop(0, nq, kv_step, 0)

    pair256 = pl.BlockSpec((S, 2 * LANE), lambda b, p: (b, p))
    pair128 = pl.BlockSpec((S, LANE), lambda b, p: (b, p))
    return _pcall(body, name=name, grid=(nb, npair),
                  in_specs=[pair256, pair256, pair128, pair128, pair128, pair128],
                  out_specs=[pair256, pair256, pair128],
                  out_shape=[_sds((T, NH * LANE), F32), _sds((T, NH * LANE), F32), _sds((T, NH * NOPE), F32)],
                  scratch=[pltpu.VMEM((S, LANE), F32)], comm=comm)(q, k, v, o, lse, do)


class _NoExchange:
    def __init__(self, weights):
        self.W, self.G = weights, {}

    def rider(self, name):
        return None

    def landed(self, name, outs):
        pass

    def grad(self, key, g):
        self.G[key] = g


def _run(plan, fn, *args, name, **kw):
    rider = plan.rider(name)
    if rider is None:
        return fn(*args, name=name, **kw)
    outs, landed = fn(*args, name=name, comm=rider, **kw)
    plan.landed(name, landed)
    return outs


def _layer_fwd_bwd(x, tgt, mod3, cos, sin, plan, P):
    nb = mod3.shape[0]
    W = plan.W
    h1 = _fwd_block(x, None, mod3, P["norm_ffn1"], sub=0, coef=0.0, name="fwd_norm1")
    gu1, a1 = _run(plan, _ffn_up, h1, W["ffn1_in"], name="ffn1_up")
    f1 = _mm(a1, W["ffn1_out"], mode="nn", out_dtype=F32, name="ffn1_out", tm=1024, tn=1024, tk=DFF)
    x1, h2 = _fwd_block(x, f1, mod3, P["norm_mix"], sub=1, coef=0.5, name="fwd_norm2")
    z = _mm(h2, W["w_in"], mode="nn", out_dtype=F32, name="mix_in", tn=1664)
    pooled, mixed, scaled = _pool_fwd(z, W["pool_grp"], P["pool_scale"], nb=nb, name="pool_fwd")
    br_pool = _mm(scaled, W["pool_proj"], mode="nn", out_dtype=F32, name="pool_proj")
    qn, kvn = _lat_fwd(z, P["q_a_norm"], P["kv_a_norm"], name="lat_fwd")
    qp = _mm(qn, W["q_up"], mode="nn", out_dtype=F32, name="q_up")
    kvp = _mm(kvn, W["kv_up"], mode="nn", out_dtype=F32, name="kv_up")
    q, k, v = _prep_fwd(qp, kvp, z, cos, sin, P["q_gain"], P["k_gain"], name="prep_fwd")
    attn, lse = _run(plan, _attn_fwd, q, k, v, nb=nb, name="attn_fwd")
    br_mla = _mm(attn, W["mla_proj"], mode="nn", out_dtype=F32, name="mla_proj")
    merged = _merge_fwd(z, br_pool, br_mla, name="merge_fwd")
    mo = _mm(merged, W["w_out"], mode="nn", out_dtype=F32, name="mix_out")
    x2, h3 = _fwd_block(x1, mo, mod3, P["norm_ffn2"], sub=2, coef=1.0, name="fwd_norm3")
    gu2, a2 = _ffn_up(h3, W["ffn2_in"], name="ffn2_up")
    f2 = _mm(a2, W["ffn2_out"], mode="nn", out_dtype=F32, name="ffn2_out", tm=1024, tn=1024, tk=DFF)
    dy, df2, st_fin, loss = _final(x2, f2, tgt, mod3, name="loss_head")

    plan.grad("ffn2_out", _dw(a2, df2, name="d_ffn2_out"))
    dgu2 = _ffn_dact(df2, gu2, W["ffn2_out"], name="ffn2_dact")
    dh3 = _ffn_dh(dgu2, W["ffn2_in"], name="d_ffn2_h")
    plan.grad("ffn2_in", _ffn_dw_in(h3, dgu2, name="d_ffn2_in"))
    dx2, dmo, st3 = _run(plan, _bwd_block, x2, dh3, dy, mo, mod3, P["norm_ffn2"], sub=2, coef=1.0, name="bwd_norm3")
    plan.grad("w_out", _dw(merged, dmo, name="d_mix_out"))
    dmerged = _mm(dmo, W["w_out"], mode="nt", out_dtype=F32, name="d_merged")
    dbr_pool, dbr_mla, dgates = _merge_bwd(z, br_pool, br_mla, dmerged, name="merge_bwd")
    plan.grad("pool_proj", _dw(scaled, dbr_pool, name="d_pool_proj"))
    dscaled = _mm(dbr_pool, W["pool_proj"], mode="nt", out_dtype=F32, name="d_pool_scaled")
    du_pool, d_pool_grp, d_pool_scale = _pool_bwd(dscaled, mixed, pooled, W["pool_grp"], P["pool_scale"], nb=nb, name="pool_bwd")
    plan.grad("mla_proj", _dw(attn, dbr_mla, name="d_mla_proj"))
    dattn = _mm(dbr_mla, W["mla_proj"], mode="nt", out_dtype=F32, name="d_attn")
    dq, dk, dv = _run(plan, _attn_bwd, q, k, v, attn, lse, dattn, nb=nb, name="attn_bwd")
    dqp, dkvp, dkr, st_prep = _prep_bwd(dq, dk, dv, qp, kvp, z, cos, sin, P["q_gain"], P["k_gain"], nb=nb, name="prep_bwd")
    plan.grad("q_up", _dw(qn, dqp, name="d_q_up"))
    dqn = _mm(dqp, W["q_up"], mode="nt", out_dtype=F32, name="d_qn")
    plan.grad("kv_up", _dw(kvn, dkvp, name="d_kv_up"))
    dkvn = _mm(dkvp, W["kv_up"], mode="nt", out_dtype=F32, name="d_kvn")
    dql, dkvl, st_q, st_kv = _lat_bwd(z, dqn, dkvn, P["q_a_norm"], P["kv_a_norm"], nb=nb, name="lat_bwd")
    dz = jnp.concatenate([du_pool, dql, dkvl, dkr, dgates], axis=1)
    plan.grad("w_in", _dw(h2, dz, name="d_mix_in", tn=1664))
    dh2 = _run(plan, _mm, dz, W["w_in"], mode="nt", out_dtype=F32, name="d_mix_h")
    dx1, df1, st2 = _bwd_block(x1, dh2, dx2, f1, mod3, P["norm_mix"], sub=1, coef=0.5, name="bwd_norm2")
    plan.grad("ffn1_out", _dw(a1, df1, name="d_ffn1_out"))
    dgu1 = _run(plan, _ffn_dact, df1, gu1, W["ffn1_out"], name="ffn1_dact")
    dh1 = _run(plan, _ffn_dh, dgu1, W["ffn1_in"], name="d_ffn1_h")
    plan.grad("ffn1_in", _run(plan, _ffn_dw_in, h1, dgu1, name="d_ffn1_in"))
    grad_x, st1 = _run(plan, _bwd_block, x, dh1, dx1, None, mod3, P["norm_ffn1"], sub=0, coef=0.0, name="bwd_norm1")

    dmod = jnp.stack([st1[:, 0], st1[:, 1], st2[:, 3], st2[:, 0], st2[:, 1], st3[:, 3], st3[:, 0], st3[:, 1], st_fin[:, 0]],
                     axis=1)
    small = {
        "norm_ffn1": st1[:, 2], "norm_mix": st2[:, 2], "norm_ffn2": st3[:, 2],
        "pool_grp": d_pool_grp, "pool_scale": d_pool_scale[:, 0],
        "q_a_norm": st_q[:, 0], "kv_a_norm": st_kv[:, 0],
        "q_gain": st_prep[:, 0], "k_gain": st_prep[:, 1],
    }
    return loss, grad_x, small, dmod


def _w_in_to_kernel(w):
    k = w.shape[0]
    zeros = lambda n: jnp.zeros((k, n), w.dtype)
    return jnp.concatenate([w[:, 0:1152], zeros(KR_LANE), w[:, 1152:1184], zeros(LANE - KR_LANE - ROPE), w[:, 1184:]], axis=1)


def _w_in_from_kernel(g):
    return jnp.concatenate([g[:, 0:1152], g[:, Z_KR + KR_LANE:Z_KR + KR_LANE + ROPE], g[:, Z_GP:]], axis=1)


def _q_up_to_kernel(w):
    k = w.shape[0]
    return jnp.pad(w.reshape(k, NH, NOPE + ROPE), ((0, 0), (0, 0), (0, LANE - NOPE - ROPE))).reshape(k, NH * LANE)


def _q_up_from_kernel(g):
    k = g.shape[0]
    return g.reshape(k, NH, LANE)[:, :, :NOPE + ROPE].reshape(k, NH * (NOPE + ROPE))


def _kv_up_to_kernel(w):
    k = w.shape[0]
    w3 = w.reshape(k, NH, 2 * NOPE)
    kpart = jnp.pad(w3[:, :, :NOPE], ((0, 0), (0, 0), (0, LANE - NOPE))).reshape(k, NH * LANE)
    return jnp.concatenate([kpart, w3[:, :, NOPE:].reshape(k, NH * NOPE)], axis=1)


def _kv_up_from_kernel(g):
    k = g.shape[0]
    kpart = g[:, :NH * LANE].reshape(k, NH, LANE)[:, :, :NOPE]
    vpart = g[:, NH * LANE:].reshape(k, NH, NOPE)
    return jnp.concatenate([kpart, vpart], axis=2).reshape(k, NH * 2 * NOPE)


def _gain_slab(nope, rope):
    return jnp.concatenate([nope, rope, jnp.zeros((1, LANE - NOPE - ROPE), nope.dtype)], axis=1)


def _rope_tables(positions):
    inv_freq = 10000.0 ** (-jnp.arange(0, ROPE, 2, dtype=F32) / ROPE)
    ang = positions.astype(F32)[:, None] * inv_freq
    ang = jnp.concatenate([ang, ang], axis=-1)
    t = positions.shape[0]
    cos = jnp.concatenate([jnp.ones((t, KR_LANE), F32), jnp.cos(ang), jnp.ones((t, LANE - KR_LANE - ROPE), F32)], axis=1)
    sin = jnp.concatenate([jnp.zeros((t, KR_LANE), F32), jnp.sin(ang), jnp.zeros((t, LANE - KR_LANE - ROPE), F32)], axis=1)
    return cos, sin


def _coords():
    return lax.axis_index("x"), lax.axis_index("y"), lax.axis_index("c")


HBM_SPEC = pl.BlockSpec(memory_space=pl.ANY)
VMEM_SPEC = pl.BlockSpec(memory_space=pltpu.VMEM)


def _gather8(v, *, name, comm=None):
    r, c = v.shape

    def body(v_ref, out_ref, send_sems, recv_sems, local_sem):
        x, y, cc = _coords()
        me = 4 * x + 2 * y + cc
        mine = pltpu.make_async_copy(v_ref, out_ref.at[me], local_sem)
        mine.start()
        copies = []
        for kk in range(1, N_DEV):
            peer = (x ^ (kk >> 2), y ^ ((kk >> 1) & 1), cc ^ (kk & 1))
            cp = pltpu.make_async_remote_copy(src_ref=v_ref, dst_ref=out_ref.at[me], send_sem=send_sems.at[kk - 1],
                                              recv_sem=recv_sems.at[kk - 1], device_id=peer, device_id_type=MESH)
            cp.start()
            copies.append(cp)
        for kk in range(1, N_DEV):
            peer_slot = me ^ kk
            pltpu.make_async_remote_copy(src_ref=v_ref, dst_ref=out_ref.at[peer_slot], send_sem=send_sems.at[kk - 1],
                                         recv_sem=recv_sems.at[kk - 1], device_id=(x, y, cc), device_id_type=MESH).wait_recv()
        for cp in copies:
            cp.wait_send()
        mine.wait()

    return _pcall(body, name=name, out_shape=_sds((N_DEV, r, c), v.dtype), in_specs=[VMEM_SPEC], out_specs=VMEM_SPEC,
                  scratch=[pltpu.SemaphoreType.DMA((N_DEV - 1,)), pltpu.SemaphoreType.DMA((N_DEV - 1,)), pltpu.SemaphoreType.DMA],
                  comm=comm)(v)


CHIP_RELS = ((1, 0), (0, 1), (1, 1))


def _comm_only(comm, *, name):
    _, outs = _pcall(lambda: None, name=name, out_shape=[], in_specs=[], out_specs=[], comm=comm)()
    return outs


def _gather_comm(shards):
    n = len(shards)

    def ici(ins, outs, sems, a, j, x, y, cc):
        half = ins[a].shape[0] // 2
        mine = pl.ds(cc * half, half)
        dx, dy = CHIP_RELS[j]
        return pltpu.make_async_remote_copy(src_ref=ins[a].at[mine], dst_ref=outs[a].at[2 * x + y, mine],
                                            send_sem=sems[0].at[a, j], recv_sem=sems[1].at[a, j],
                                            device_id=(x ^ dx, y ^ dy, cc), device_id_type=MESH)

    def d2d(ins, outs, sems, a, j, x, y, cc, half_of):
        half = ins[a].shape[0] // 2
        dx, dy = CHIP_RELS[j]
        landed = outs[a].at[2 * (x ^ dx) + (y ^ dy), pl.ds(half_of * half, half)]
        return pltpu.make_async_remote_copy(src_ref=landed, dst_ref=landed, send_sem=sems[2].at[a, j], recv_sem=sems[3].at[a, j],
                                            device_id=(x, y, 1 - cc), device_id_type=MESH)

    def own(ins, outs, sems, a, x, y):
        return pltpu.make_async_copy(ins[a], outs[a].at[2 * x + y], sems[4].at[a])

    def start(ins, outs, sems):
        x, y, cc = _coords()
        for a in range(n):
            own(ins, outs, sems, a, x, y).start()
            for j in range(3):
                ici(ins, outs, sems, a, j, x, y, cc).start()

    def finish(ins, outs, sems):
        x, y, cc = _coords()
        for a in range(n):
            for j in range(3):
                ici(ins, outs, sems, a, j, x, y, cc).wait_recv()
                d2d(ins, outs, sems, a, j, x, y, cc, cc).start()
        for a in range(n):
            for j in range(3):
                d2d(ins, outs, sems, a, j, x, y, cc, 1 - cc).wait_recv()
        for a in range(n):
            for j in range(3):
                ici(ins, outs, sems, a, j, x, y, cc).wait_send()
                d2d(ins, outs, sems, a, j, x, y, cc, cc).wait_send()
            own(ins, outs, sems, a, x, y).wait()

    dma = pltpu.SemaphoreType.DMA
    return _Comm(shards, [_sds((N_CHIP,) + s.shape, s.dtype) for s in shards],
                 [dma((n, 3)), dma((n, 3)), dma((n, 3)), dma((n, 3)), dma((n,))], start, finish)


def _swap_comm(parts):
    n = len(parts)

    def copy(ins, outs, sems, a):
        x, y, cc = _coords()
        half = ins[a].shape[1] // 2
        return pltpu.make_async_remote_copy(src_ref=ins[a].at[:, pl.ds((1 - cc) * half, half)], dst_ref=outs[a],
                                            send_sem=sems[0].at[a], recv_sem=sems[1].at[a], device_id=(x, y, 1 - cc),
                                            device_id_type=MESH)

    def start(ins, outs, sems):
        for a in range(n):
            copy(ins, outs, sems, a).start()

    def finish(ins, outs, sems):
        for a in range(n):
            copy(ins, outs, sems, a).wait()

    dma = pltpu.SemaphoreType.DMA
    return _Comm(parts, [_sds((p.shape[0], p.shape[1] // 2, p.shape[2]), p.dtype) for p in parts], [dma((n,)), dma((n,))],
                 start, finish)


def _add_half(full, other, cidx, *, name):
    nch, r, c = full.shape
    half = r // 2
    tr = _pick_rows(half)
    nbk = half // tr
    grid_spec = pltpu.PrefetchScalarGridSpec(
        num_scalar_prefetch=1, grid=(nch, nbk),
        in_specs=[pl.BlockSpec((1, tr, c), lambda j, i, cref: (j, cref[0] * nbk + i, 0)),
                  pl.BlockSpec((1, tr, c), lambda j, i, cref: (j, i, 0))],
        out_specs=pl.BlockSpec((1, tr, c), lambda j, i, cref: (j, i, 0)))

    def body(cref, a_ref, b_ref, o_ref):
        o_ref[...] = (a_ref[...] + b_ref[...]).astype(o_ref.dtype)

    return _pcall(body, name=name, out_shape=_sds((nch, half, c), MX), grid_spec=grid_spec)(cidx, full, other)


def _pick_rows(rows, target=512):
    best = None
    for t in range(16, min(rows, target) + 1, 16):
        if rows % t == 0:
            best = t
    return rows if best is None else best


def _exchange_comm(parts):
    n = len(parts)

    def send(ins, outs, sems, a, j, x, y, cc):
        dx, dy = CHIP_RELS[j]
        return pltpu.make_async_remote_copy(src_ref=ins[a].at[2 * (x ^ dx) + (y ^ dy)], dst_ref=outs[a].at[2 * x + y],
                                            send_sem=sems[0].at[a, j], recv_sem=sems[1].at[a, j],
                                            device_id=(x ^ dx, y ^ dy, cc), device_id_type=MESH)

    def landing(ins, outs, sems, a, j, x, y, cc):
        dx, dy = CHIP_RELS[j]
        peer_chip = 2 * (x ^ dx) + (y ^ dy)
        return pltpu.make_async_remote_copy(src_ref=ins[a].at[peer_chip], dst_ref=outs[a].at[peer_chip], send_sem=sems[0].at[a, j],
                                            recv_sem=sems[1].at[a, j], device_id=(x, y, cc), device_id_type=MESH)

    def own(ins, outs, sems, a, x, y):
        return pltpu.make_async_copy(ins[a].at[2 * x + y], outs[a].at[2 * x + y], sems[2].at[a])

    def start(ins, outs, sems):
        x, y, cc = _coords()
        for a in range(n):
            own(ins, outs, sems, a, x, y).start()
            for j in range(3):
                send(ins, outs, sems, a, j, x, y, cc).start()

    def finish(ins, outs, sems):
        x, y, cc = _coords()
        for a in range(n):
            for j in range(3):
                landing(ins, outs, sems, a, j, x, y, cc).wait_recv()
        for a in range(n):
            for j in range(3):
                send(ins, outs, sems, a, j, x, y, cc).wait_send()
            own(ins, outs, sems, a, x, y).wait()

    dma = pltpu.SemaphoreType.DMA
    return _Comm(parts, [_sds(p.shape, p.dtype) for p in parts], [dma((n, 3)), dma((n, 3)), dma((n,))], start, finish)


def _sum_chips(q, cidx, *, name):
    nch, h, c = q.shape
    tr = _pick_rows(h)
    nbk = h // tr
    grid_spec = pltpu.PrefetchScalarGridSpec(
        num_scalar_prefetch=1, grid=(nbk,),
        in_specs=[pl.BlockSpec((nch, tr, c), lambda i, cref: (0, i, 0))],
        out_specs=pl.BlockSpec((tr, c), lambda i, cref: (cref[0] * nbk + i, 0)))

    def body(cref, q_ref, o_ref):
        acc = q_ref[0].astype(F32) + q_ref[1].astype(F32)
        acc = acc + q_ref[2].astype(F32)
        o_ref[...] = acc + q_ref[3].astype(F32)

    return _pcall(body, name=name, out_shape=_sds((2 * h, c), F32), grid_spec=grid_spec)(cidx, q)


def _join_comm(fulls):
    n = len(fulls)

    def half_copy(outs, sems, a, which):
        x, y, cc = _coords()
        h = outs[a].shape[0] // 2
        rows = outs[a].at[pl.ds((cc if which == 0 else 1 - cc) * h, h)]
        return pltpu.make_async_remote_copy(src_ref=rows, dst_ref=rows, send_sem=sems[0].at[a], recv_sem=sems[1].at[a],
                                            device_id=(x, y, 1 - cc), device_id_type=MESH)

    def start(ins, outs, sems):
        for a in range(n):
            half_copy(outs, sems, a, 0).start()

    def finish(ins, outs, sems):
        for a in range(n):
            half_copy(outs, sems, a, 1).wait_recv()
        for a in range(n):
            half_copy(outs, sems, a, 0).wait_send()

    dma = pltpu.SemaphoreType.DMA
    return _Comm(fulls, [_sds(p.shape, p.dtype) for p in fulls], [dma((n,)), dma((n,))], start, finish,
                 aliases={a: a for a in range(n)})


def _ada_fwd(c_all, w, b, *, name):
    m, kdim = c_all.shape
    n = w.shape[1]
    tn = _pick(n, 1152)

    def body(c_ref, w_ref, b_ref, o_ref):
        cv = c_ref[...]
        act = (cv * _sigmoid(cv)).astype(MX)
        o_ref[...] = jnp.dot(act, w_ref[...].astype(MX), preferred_element_type=F32) + b_ref[...]

    return _pcall(body, name=name, out_shape=_sds((m, n), F32), grid=(n // tn,),
                  in_specs=[pl.BlockSpec((m, kdim), lambda j: (0, 0)), pl.BlockSpec((kdim, tn), lambda j: (0, j)),
                            pl.BlockSpec((1, tn), lambda j: (0, j))],
                  out_specs=pl.BlockSpec((m, tn), lambda j: (0, j)))(c_all, w, b)


def _ada_bwd(c_all, dmod_cols, *, name):
    m, kdim = c_all.shape
    n = dmod_cols.shape[1]
    tn = _pick(n, 1152)

    def body(c_ref, d_ref, o_ref):
        cv = c_ref[...]
        act = (cv * _sigmoid(cv)).astype(MX)
        o_ref[...] = lax.dot_general(act, d_ref[...].astype(MX), (((0,), (0,)), ((), ())), preferred_element_type=F32)

    return _pcall(body, name=name, out_shape=_sds((kdim, n), F32), grid=(n // tn,),
                  in_specs=[pl.BlockSpec((m, kdim), lambda j: (0, 0)), pl.BlockSpec((m, tn), lambda j: (0, j))],
                  out_specs=pl.BlockSpec((kdim, tn), lambda j: (0, j)))(c_all, dmod_cols)


def _sum_small(v, nseq, rows, *, name):
    n, r, c = v.shape
    extra = r - nseq * rows

    def body(v_ref, o_ref):
        acc = None
        for d in range(n):
            for s in range(nseq):
                t = v_ref[d, s * rows:(s + 1) * rows, :]
                acc = t if acc is None else acc + t
        o_ref[0:rows, :] = acc
        acc = v_ref[0, nseq * rows:, :]
        for d in range(1, n):
            acc = acc + v_ref[d, nseq * rows:, :]
        o_ref[rows:, :] = acc

    return _pcall(body, name=name, out_shape=_sds((rows + extra, c), F32), in_specs=[VMEM_SPEC], out_specs=VMEM_SPEC)(v)


def _adamw(w, g, m, v, *, name, comm=None):
    r, c = w.shape
    tr = _pick_rows(r, 256)
    tc = c if tr < r else _pick(c, 256)
    spec = pl.BlockSpec((tr, tc), lambda i, j: (i, j))
    bc1 = 1.0 / (1.0 - ADAM_B1 ** ADAM_STEP)
    bc2 = 1.0 / (1.0 - ADAM_B2 ** ADAM_STEP)

    def body(w_ref, g_ref, m_ref, v_ref, d_ref, mo_ref, vo_ref):
        gv = g_ref[...]
        mn = ADAM_B1 * m_ref[...] + (1.0 - ADAM_B1) * gv
        vn = ADAM_B2 * v_ref[...] + (1.0 - ADAM_B2) * (gv * gv)
        mo_ref[...] = mn
        vo_ref[...] = vn
        d_ref[...] = -ADAM_LR * ((mn * bc1) / (jnp.sqrt(vn * bc2) + ADAM_EPS) + ADAM_WD * w_ref[...])

    out = _sds((r, c), F32)
    return _pcall(body, name=name, out_shape=[out, out, out], grid=(r // tr, c // tc), in_specs=[spec] * 4, out_specs=[spec] * 3,
                  comm=comm)(w, g, m, v)


BIG = ("w_ffn1_in", "w_ffn1_out", "w_in", "w_pool_proj", "w_q_up", "w_kv_up", "w_mla_proj", "w_out", "w_ffn2_in", "w_ffn2_out")
ROW_SHARDED = ("w_ffn1_out", "w_out", "w_ffn2_out")
KERNEL_NAME = {"w_ffn1_in": "ffn1_in", "w_ffn1_out": "ffn1_out", "w_in": "w_in", "w_pool_proj": "pool_proj", "w_q_up": "q_up",
               "w_kv_up": "kv_up", "w_mla_proj": "mla_proj", "w_out": "w_out", "w_ffn2_in": "ffn2_in", "w_ffn2_out": "ffn2_out"}
WEIGHTS = ("w_ada", "b_ada", "norm_ffn1", "w_ffn1_in", "w_ffn1_out", "norm_mix", "w_in", "pool_grp", "pool_scale", "w_pool_proj",
           "q_a_norm", "w_q_up", "kv_a_norm", "w_kv_up", "q_norm_nope", "q_norm_rope", "k_norm_nope", "k_norm_rope", "w_mla_proj",
           "w_out", "norm_ffn2", "w_ffn2_in", "w_ffn2_out")
SMALL = ("b_ada", "norm_ffn1", "norm_mix", "pool_grp", "pool_scale", "q_a_norm", "kv_a_norm", "q_norm_nope", "q_norm_rope",
         "k_norm_nope", "k_norm_rope", "norm_ffn2")
SMALL_SEQ = tuple(n for n in SMALL if n != "pool_grp")
SLAB_W = 1024


def _assemble(name, stacked):
    if name in ROW_SHARDED:
        return stacked.reshape(stacked.shape[0] * stacked.shape[1], stacked.shape[2])
    return jnp.transpose(stacked, (1, 0, 2)).reshape(stacked.shape[1], stacked.shape[0] * stacked.shape[2])


def _split(name, full):
    if name in ROW_SHARDED:
        return full.reshape(N_CHIP, full.shape[0] // N_CHIP, full.shape[1])
    return jnp.transpose(full.reshape(full.shape[0], N_CHIP, full.shape[1] // N_CHIP), (1, 0, 2))


GU = ("w_ffn1_in", "w_ffn2_in")
NARROW = ("w_in", "w_q_up")


def _to_rows(v, width=SLAB_W):
    flat = v.reshape(-1)
    rows = -(-flat.shape[0] // width)
    return jnp.pad(flat, (0, rows * width - flat.shape[0])).reshape(rows, width)


def _pack_small(parts, names=SMALL):
    rows, spans, at = [], {}, 0
    for name in names:
        r = _to_rows(parts[name])
        spans[name] = (at, parts[name].size, parts[name].shape)
        rows.append(r)
        at += r.shape[0]
    pad = (-at) % SUBLANE
    if pad:
        rows.append(jnp.zeros((pad, SLAB_W), F32))
    return jnp.concatenate(rows, axis=0), spans


def _unpack_small(slab, spans):
    out = {}
    for name, (at, size, shape) in spans.items():
        nrow = -(-size // SLAB_W)
        out[name] = slab[at:at + nrow].reshape(-1)[:size].reshape(shape)
    return out


GATHER_ON = {"gather_ffn1": ("w_ffn1_in", "w_ffn1_out"),
             "ffn1_up": ("w_in", "w_pool_proj", "w_q_up", "w_kv_up", "w_mla_proj", "w_out"),
             "attn_fwd": ("w_ffn2_in", "w_ffn2_out")}
SWAP_ON = {"bwd_norm3": ("w_ffn2_out", "w_ffn2_in"),
           "d_mix_h": ("w_out", "w_pool_proj", "w_mla_proj", "w_q_up", "w_kv_up", "w_in"),
           "ffn1_dact": ("w_ffn1_out",),
           "bwd_norm1": ("w_ffn1_in",)}
EXCHANGE_ON = {"attn_bwd": "bwd_norm3", "d_ffn1_h": "d_mix_h", "d_ffn1_in": "ffn1_dact", "gather_small": "bwd_norm1"}


class _ExchangePlan:
    def __init__(self, shards, cidx):
        self.shards, self.cidx = shards, cidx
        self.W, self.G, self.parts, self.pre, self.reduced = {}, {}, {}, {}, {}

    def grad(self, key, g):
        self.G[key] = g

    def rider(self, name):
        if name in GATHER_ON:
            return _gather_comm([self.shards[n] for n in GATHER_ON[name]])
        if name in SWAP_ON:
            self.parts[name] = [self._stacked(n) for n in SWAP_ON[name]]
            return _swap_comm(self.parts[name])
        if name in EXCHANGE_ON:
            return _exchange_comm(self.pre[EXCHANGE_ON[name]])
        return None

    def landed(self, name, outs):
        if name in GATHER_ON:
            for n, g in zip(GATHER_ON[name], outs):
                self.W[KERNEL_NAME[n]] = self._to_kernel(n, g)
        elif name in SWAP_ON:
            self.pre[name] = [_add_half(p, o, self.cidx, name="rs_add_" + KERNEL_NAME[n])
                              for n, p, o in zip(SWAP_ON[name], self.parts[name], outs)]
        elif name in EXCHANGE_ON:
            for n, q in zip(SWAP_ON[EXCHANGE_ON[name]], outs):
                self.reduced[n] = _sum_chips(q, self.cidx, name="rs_sum_" + KERNEL_NAME[n])

    @staticmethod
    def _to_kernel(n, stacked):
        if n in GU:
            return stacked
        full = _assemble(n, stacked)
        return {"w_in": _w_in_to_kernel, "w_q_up": _q_up_to_kernel, "w_kv_up": _kv_up_to_kernel}.get(n, lambda w: w)(full)

    def _stacked(self, n):
        g = self.G[KERNEL_NAME[n]]
        if n in GU:
            return g
        full = {"w_in": _w_in_from_kernel, "w_q_up": _q_up_from_kernel, "w_kv_up": _kv_up_from_kernel}.get(n, lambda w: w)(g)
        return _split(n, full)


def kernel(x, c, positions, w_ada, b_ada, norm_ffn1, w_ffn1_in, w_ffn1_out, norm_mix, w_in, pool_grp, pool_scale, w_pool_proj, q_a_norm, w_q_up, kv_a_norm, w_kv_up, q_norm_nope, q_norm_rope, k_norm_nope, k_norm_rope, w_mla_proj, w_out, norm_ffn2, w_ffn2_in, w_ffn2_out, loss_target, m_w_ada, m_b_ada, m_norm_ffn1, m_w_ffn1_in, m_w_ffn1_out, m_norm_mix, m_w_in, m_pool_grp, m_pool_scale, m_w_pool_proj, m_q_a_norm, m_w_q_up, m_kv_a_norm, m_w_kv_up, m_q_norm_nope, m_q_norm_rope, m_k_norm_nope, m_k_norm_rope, m_w_mla_proj, m_w_out, m_norm_ffn2, m_w_ffn2_in, m_w_ffn2_out, v_w_ada, v_b_ada, v_norm_ffn1, v_w_ffn1_in, v_w_ffn1_out, v_norm_mix, v_w_in, v_pool_grp, v_pool_scale, v_w_pool_proj, v_q_a_norm, v_w_q_up, v_kv_a_norm, v_w_kv_up, v_q_norm_nope, v_q_norm_rope, v_k_norm_nope, v_k_norm_rope, v_w_mla_proj, v_w_out, v_norm_ffn2, v_w_ffn2_in, v_w_ffn2_out):
    args = dict(locals())
    wts = {n: args[n][0] for n in WEIGHTS}
    mom = {n: args["m_" + n][0] for n in WEIGHTS}
    var = {n: args["v_" + n][0] for n in WEIGHTS}
    nb, seq, dm = x.shape
    tokens = nb * seq
    xi, yi, ci = _coords()
    chip = 2 * xi + yi
    dev = 4 * xi + 2 * yi + ci

    c_all = _gather8(c, name="gather_c").reshape(N_DEV * nb, dm)
    ncol = w_ada.shape[2]
    b_cols = lax.dynamic_slice_in_dim(wts["b_ada"].reshape(1, -1), chip * ncol, ncol, axis=1)
    mod_cols = _ada_fwd(c_all, wts["w_ada"], b_cols, name="ada_fwd")
    mod_all = _gather8(mod_cols, name="gather_mod")
    mine = lax.dynamic_slice_in_dim(mod_all, dev * nb, nb, axis=1)
    mod = jnp.concatenate([mine[0], mine[2], mine[4], mine[6]], axis=1)
    mod3 = mod.reshape(nb, 9, dm)

    cidx = ci.astype(jnp.int32).reshape(1)
    plan = _ExchangePlan({n: wts[n].astype(MX) for n in BIG}, cidx)
    plan.W["pool_grp"] = wts["pool_grp"].astype(MX)
    plan.landed("gather_ffn1", _comm_only(plan.rider("gather_ffn1"), name="gather_ffn1"))
    P = {"norm_ffn1": wts["norm_ffn1"].reshape(1, dm), "norm_mix": wts["norm_mix"].reshape(1, dm),
         "norm_ffn2": wts["norm_ffn2"].reshape(1, dm), "pool_scale": wts["pool_scale"].reshape(1, POOL_W),
         "q_a_norm": wts["q_a_norm"].reshape(1, QL), "kv_a_norm": wts["kv_a_norm"].reshape(1, KVL),
         "q_gain": _gain_slab(wts["q_norm_nope"].reshape(1, NOPE), wts["q_norm_rope"].reshape(1, ROPE)),
         "k_gain": _gain_slab(wts["k_norm_nope"].reshape(1, NOPE), wts["k_norm_rope"].reshape(1, ROPE))}
    cos, sin = _rope_tables(positions.reshape(tokens))

    loss8, grad_x, small, dmod = _layer_fwd_bwd(x.reshape(tokens, dm), loss_target.reshape(tokens, dm), mod3, cos, sin, plan, P)
    loss = lax.psum(loss8[0, 0], ("x", "y", "c"))


    qg, kg = small["q_gain"], small["k_gain"]
    per_seq = {"b_ada": dmod.reshape(nb, 9 * dm), "norm_ffn1": small["norm_ffn1"], "norm_mix": small["norm_mix"],
               "pool_scale": small["pool_scale"], "q_a_norm": small["q_a_norm"],
               "kv_a_norm": small["kv_a_norm"], "q_norm_nope": qg[:, :NOPE], "q_norm_rope": qg[:, NOPE:NOPE + ROPE],
               "k_norm_nope": kg[:, :NOPE], "k_norm_rope": kg[:, KR_LANE:KR_LANE + ROPE], "norm_ffn2": small["norm_ffn2"]}
    slabs, spans = [], None
    for s in range(nb):
        slab, spans = _pack_small({n: per_seq[n][s] for n in SMALL_SEQ}, SMALL_SEQ)
        slabs.append(slab)
    rows = slabs[0].shape[0]
    slabs.append(_to_rows(small["pool_grp"]))
    gathered_small = _run(plan, _gather8, jnp.concatenate(slabs, axis=0), name="gather_small")
    joined = _comm_only(_join_comm([plan.reduced[n] for n in BIG]), name="rs_join_halves")
    grads = {n: g for n, g in zip(BIG, joined)}
    small_sum = _sum_small(gathered_small, nb, rows, name="sum_small")
    small_grads = _unpack_small(small_sum[:rows], spans)
    small_grads["pool_grp"] = small_sum[rows:]
    for n in SMALL:
        grads[n] = small_grads[n].reshape(wts[n].shape)
    at, _, _ = spans["b_ada"]
    nrow_b = 9 * dm // SLAB_W
    dmod_all = gathered_small[:, :nb * rows].reshape(N_DEV * nb, rows, SLAB_W)[:, at:at + nrow_b].reshape(N_DEV * nb, 9 * dm)
    dmod_cols = lax.dynamic_slice_in_dim(dmod_all, chip * ncol, ncol, axis=1)
    grads["w_ada"] = _ada_bwd(c_all, dmod_cols, name="ada_bwd")

    delta, new_m, new_v = {}, {}, {}
    for n in ("w_ada",) + BIG:
        if n in NARROW:
            res = _adamw(wts[n].T, grads[n].T, mom[n].T, var[n].T, name="adamw_" + n)
            delta[n], new_m[n], new_v[n] = (r.T for r in res)
        else:
            delta[n], new_m[n], new_v[n] = _adamw(wts[n], grads[n], mom[n], var[n], name="adamw_" + n)
    w_slab, sp = _pack_small({n: wts[n] for n in SMALL})
    g_slab, _ = _pack_small({n: grads[n] for n in SMALL})
    m_slab, _ = _pack_small({n: mom[n] for n in SMALL})
    v_slab, _ = _pack_small({n: var[n] for n in SMALL})
    d_s, m_s, v_s = _adamw(w_slab, g_slab, m_slab, v_slab, name="adamw_small")
    for dst, slab in ((delta, d_s), (new_m, m_s), (new_v, v_s)):
        un = _unpack_small(slab, sp)
        for n in SMALL:
            dst[n] = un[n]

    def lead(a, n):
        return a.reshape((1,) + wts[n].shape)

    return (loss, grad_x.reshape(nb, seq, dm), *[lead(grads[n], n) for n in WEIGHTS], *[lead(delta[n], n) for n in WEIGHTS],
            *[lead(new_m[n], n) for n in WEIGHTS], *[lead(new_v[n], n) for n in WEIGHTS])
```

```python
import functools
import math

import jax
import jax.numpy as jnp
from jax import lax
from jax.experimental import pallas as pl
from jax.experimental.pallas import tpu as pltpu

F32 = jnp.float32
MX = jnp.bfloat16

D = 1024
DFF = 2816
NH = 8
POOL_W = 512
POOL_G = 4
QL = 384
KVL = 256
ROPE = 32
NOPE = 64
LANE = 128
SUBLANE = 8
EPS = 1e-6
ATTN_SCALE = 1.0 / math.sqrt(96.0)
NEG = -1e30

Z_UP, Z_QL, Z_KV, Z_KR, Z_GP, Z_GM, Z_W = 0, 512, 896, 1152, 1280, 2304, 3328
KR_LANE = 64
LAT_W = 1280

ADAM_LR, ADAM_B1, ADAM_B2, ADAM_EPS, ADAM_WD, ADAM_STEP = 0.001, 0.9, 0.999, 1e-08, 0.01, 10

VMEM_LIMIT = 48 * 1024 * 1024
MESH = pl.DeviceIdType.MESH
N_DEV = 8
N_CHIP = 4


class _Comm:
    def __init__(self, ins, out_shapes, sems, start, finish, aliases=None):
        self.ins, self.out_shapes, self.sems = list(ins), list(out_shapes), list(sems)
        self.start, self.finish = start, finish
        self.aliases = aliases or {}


def _pcall(body, *, name, out_shape, grid=(), in_specs=None, out_specs=None, scratch=(), grid_spec=None, aliases=None,
           comm=None):
    params = pltpu.CompilerParams(vmem_limit_bytes=VMEM_LIMIT)
    kw = dict(name=name, compiler_params=params)
    if comm is None:
        if aliases:
            kw["input_output_aliases"] = aliases
        if grid_spec is not None:
            return pl.pallas_call(body, grid_spec=grid_spec, out_shape=out_shape, **kw)
        return pl.pallas_call(body, grid=grid, in_specs=in_specs, out_specs=out_specs, scratch_shapes=scratch,
                              out_shape=out_shape, **kw)
    single = not isinstance(out_shape, (list, tuple))
    outs = [out_shape] if single else list(out_shape)
    ospecs = [out_specs] if single else list(out_specs)
    n_in, n_out, n_ci, n_co, n_scr = len(in_specs), len(outs), len(comm.ins), len(comm.out_shapes), len(scratch)
    io = dict(aliases or {})
    io.update({n_in + i: n_out + o for i, o in comm.aliases.items()})

    def riding(*refs):
        ins, cins = refs[:n_in], refs[n_in:n_in + n_ci]
        at = n_in + n_ci
        os_, couts = refs[at:at + n_out], refs[at + n_out:at + n_out + n_co]
        at += n_out + n_co
        scr, csems = refs[at:at + n_scr], refs[at + n_scr:]
        if grid:
            first = functools.reduce(jnp.logical_and, [pl.program_id(d) == 0 for d in range(len(grid))])
            last = functools.reduce(jnp.logical_and, [pl.program_id(d) == grid[d] - 1 for d in range(len(grid))])
            pl.when(first)(lambda: comm.start(cins, couts, csems))
            body(*ins, *os_, *scr)
            pl.when(last)(lambda: comm.finish(cins, couts, csems))
        else:
            comm.start(cins, couts, csems)
            body(*ins, *os_, *scr)
            comm.finish(cins, couts, csems)

    call = pl.pallas_call(riding, grid=grid, in_specs=list(in_specs) + [HBM_SPEC] * n_ci, out_specs=ospecs + [HBM_SPEC] * n_co,
                          out_shape=outs + comm.out_shapes, scratch_shapes=list(scratch) + comm.sems,
                          input_output_aliases=io, **kw)

    def run(*args):
        res = call(*args, *comm.ins)
        main = list(res[:n_out])
        return (main[0] if single else main), list(res[n_out:])

    return run


def _pick(dim, target):
    best = None
    for t in range(LANE, min(dim, target) + 1, LANE):
        if dim % t == 0:
            best = t
    return dim if best is None else best


def _sds(shape, dtype):
    return jax.ShapeDtypeStruct(shape, dtype)


def _dw(a, g, *, name, tn=1024):
    return _mm(a, g, mode="tn", out_dtype=F32, name=name, tm=256, tn=tn, tk=a.shape[0], n_outer=True)


def _mm(a, b, *, mode, out_dtype, name, tm=1024, tn=1024, tk=4096, n_outer=False, comm=None):
    if mode == "nn":
        (M, K), (K2, N) = a.shape, b.shape
    elif mode == "nt":
        (M, K), (N, K2) = a.shape, b.shape
    else:
        (K, M), (K2, N) = a.shape, b.shape
    assert K == K2, (name, a.shape, b.shape)
    tm, tn, tk = _pick(M, tm), _pick(N, tn), _pick(K, tk)
    nk = K // tk
    if n_outer:
        ij = lambda g0, g1: (g1, g0)
        grid = (N // tn, M // tm, nk)
    else:
        ij = lambda g0, g1: (g0, g1)
        grid = (M // tm, N // tn, nk)
    if mode == "tn":
        a_spec = pl.BlockSpec((tk, tm), lambda g0, g1, k: (k, ij(g0, g1)[0]))
    else:
        a_spec = pl.BlockSpec((tm, tk), lambda g0, g1, k: (ij(g0, g1)[0], k))
    if mode == "nt":
        b_spec = pl.BlockSpec((tn, tk), lambda g0, g1, k: (ij(g0, g1)[1], k))
    else:
        b_spec = pl.BlockSpec((tk, tn), lambda g0, g1, k: (k, ij(g0, g1)[1]))
    o_spec = pl.BlockSpec((tm, tn), lambda g0, g1, k: ij(g0, g1))
    dn = {"nn": (((1,), (0,)), ((), ())), "nt": (((1,), (1,)), ((), ())), "tn": (((0,), (0,)), ((), ()))}[mode]

    def dot(a_ref, b_ref):
        return lax.dot_general(a_ref[...].astype(MX), b_ref[...].astype(MX), dn, preferred_element_type=F32)

    def body_one(a_ref, b_ref, o_ref):
        o_ref[...] = dot(a_ref, b_ref).astype(o_ref.dtype)

    def body_acc(a_ref, b_ref, o_ref, acc_ref):
        k = pl.program_id(2)
        part = dot(a_ref, b_ref)

        @pl.when(k == 0)
        def _():
            acc_ref[...] = part

        @pl.when(k > 0)
        def _():
            acc_ref[...] += part

        @pl.when(k == nk - 1)
        def _():
            o_ref[...] = acc_ref[...].astype(o_ref.dtype)

    return _pcall(body_one if nk == 1 else body_acc, name=name, out_shape=_sds((M, N), out_dtype), grid=grid,
                  in_specs=[a_spec, b_spec], out_specs=o_spec, scratch=[] if nk == 1 else [pltpu.VMEM((tm, tn), F32)],
                  comm=comm)(a, b)


def _gu_shard(q):
    return (q % 2) * 2 + q // 2


def _ffn_up(h, w_st, *, name, tm=512, comm=None):
    T, dm = h.shape
    hw = w_st.shape[2]
    tm = _pick(T, tm)

    def body(h_ref, wg_ref, wu_ref, gu_ref, a_ref):
        hv = h_ref[...]
        g = jnp.dot(hv, wg_ref[0], preferred_element_type=F32)
        u = jnp.dot(hv, wu_ref[0], preferred_element_type=F32)
        gu_ref[:, :hw] = g.astype(gu_ref.dtype)
        gu_ref[:, hw:] = u.astype(gu_ref.dtype)
        a_ref[...] = (g * _sigmoid(g) * u).astype(a_ref.dtype)

    return _pcall(body, name=name, grid=(T // tm, 2),
                  in_specs=[pl.BlockSpec((tm, dm), lambda i, j: (i, 0)), pl.BlockSpec((1, dm, hw), lambda i, j: (j, 0, 0)),
                            pl.BlockSpec((1, dm, hw), lambda i, j: (2 + j, 0, 0))],
                  out_specs=[pl.BlockSpec((tm, 2 * hw), lambda i, j: (i, j)), pl.BlockSpec((tm, hw), lambda i, j: (i, j))],
                  out_shape=[_sds((T, 4 * hw), MX), _sds((T, 2 * hw), MX)], comm=comm)(h, w_st, w_st)


def _ffn_dact(df, gu, w_out, *, name, tm=512, comm=None):
    T, dm = df.shape
    hw = gu.shape[1] // 4
    tm = _pick(T, tm)

    def body(df_ref, gu_ref, wo_ref, dgu_ref):
        da = lax.dot_general(df_ref[...], wo_ref[...], (((1,), (1,)), ((), ())), preferred_element_type=F32)
        g = gu_ref[:, :hw].astype(F32)
        u = gu_ref[:, hw:].astype(F32)
        s = _sigmoid(g)
        dgu_ref[:, :hw] = (da * u * (s * (1.0 + g * (1.0 - s)))).astype(dgu_ref.dtype)
        dgu_ref[:, hw:] = (da * (g * s)).astype(dgu_ref.dtype)

    return _pcall(body, name=name, grid=(T // tm, 2),
                  in_specs=[pl.BlockSpec((tm, dm), lambda i, j: (i, 0)), pl.BlockSpec((tm, 2 * hw), lambda i, j: (i, j)),
                            pl.BlockSpec((hw, dm), lambda i, j: (j, 0))],
                  out_specs=pl.BlockSpec((tm, 2 * hw), lambda i, j: (i, j)), out_shape=_sds(gu.shape, MX),
                  comm=comm)(df, gu, w_out)


def _ffn_dh(dgu, w_st, *, name, tm=1024, comm=None):
    T = dgu.shape[0]
    _, dm, hw = w_st.shape
    tm = _pick(T, tm)

    def body(d_ref, w_ref, o_ref, acc_ref):
        q = pl.program_id(1)
        part = lax.dot_general(d_ref[...], w_ref[0], (((1,), (1,)), ((), ())), preferred_element_type=F32)

        @pl.when(q == 0)
        def _():
            acc_ref[...] = part

        @pl.when(q > 0)
        def _():
            acc_ref[...] += part

        @pl.when(q == 3)
        def _():
            o_ref[...] = acc_ref[...]

    return _pcall(body, name=name, grid=(T // tm, 4),
                  in_specs=[pl.BlockSpec((tm, hw), lambda i, q: (i, q)), pl.BlockSpec((1, dm, hw), lambda i, q: (_gu_shard(q), 0, 0))],
                  out_specs=pl.BlockSpec((tm, dm), lambda i, q: (i, 0)), out_shape=_sds((T, dm), F32),
                  scratch=[pltpu.VMEM((tm, dm), F32)], comm=comm)(dgu, w_st)


def _ffn_dw_in(h, dgu, *, name, tm=256, comm=None):
    T, dm = h.shape
    hw = dgu.shape[1] // 4
    tm = _pick(dm, tm)

    def body(h_ref, d_ref, o_ref):
        o_ref[0] = lax.dot_general(h_ref[...], d_ref[...], (((0,), (0,)), ((), ())), preferred_element_type=F32)

    return _pcall(body, name=name, grid=(4, dm // tm),
                  in_specs=[pl.BlockSpec((T, tm), lambda q, i: (0, i)), pl.BlockSpec((T, hw), lambda q, i: (0, q))],
                  out_specs=pl.BlockSpec((1, tm, hw), lambda q, i: (_gu_shard(q), i, 0)),
                  out_shape=_sds((4, dm, hw), F32), comm=comm)(h, dgu)


def _rsq(x):
    return lax.rsqrt(jnp.mean(x * x, axis=-1, keepdims=True) + EPS)


def _sigmoid(x):
    return 1.0 / (1.0 + jnp.exp(-x))


def _row_spec(tm, w):
    return pl.BlockSpec((tm, w), lambda i: (i, 0))


def _fwd_block(x_prev, f_prev, mod3, gain, *, sub, coef, name, tm=256):
    T, dm = x_prev.shape
    tps = (T // mod3.shape[0]) // tm
    has_f = f_prev is not None
    mod_spec = pl.BlockSpec((1, 9, dm), lambda i: (i // tps, 0, 0))
    vec_spec = pl.BlockSpec((1, dm), lambda i: (0, 0))

    def body(*refs):
        if has_f:
            x_ref, f_ref, mod_ref, n_ref, xo_ref, h_ref = refs
            x = x_ref[...] + coef * mod_ref[0, 3 * sub - 1:3 * sub, :] * f_ref[...]
            xo_ref[...] = x
        else:
            x_ref, mod_ref, n_ref, h_ref = refs
            x = x_ref[...]
        xn = x * _rsq(x) * n_ref[...]
        h = xn * (1.0 + mod_ref[0, 3 * sub + 1:3 * sub + 2, :]) + mod_ref[0, 3 * sub:3 * sub + 1, :]
        h_ref[...] = h.astype(h_ref.dtype)

    row = _row_spec(tm, dm)
    if has_f:
        return _pcall(body, name=name, grid=(T // tm,), in_specs=[row, row, mod_spec, vec_spec], out_specs=[row, row],
                      out_shape=[_sds((T, dm), F32), _sds((T, dm), MX)])(x_prev, f_prev, mod3, gain)
    return _pcall(body, name=name, grid=(T // tm,), in_specs=[row, mod_spec, vec_spec], out_specs=row,
                  out_shape=_sds((T, dm), MX))(x_prev, mod3, gain)


def _final(x2, f2, tgt, mod3, *, name, tm=256):
    T, dm = x2.shape
    tps = (T // mod3.shape[0]) // tm
    mod_spec = pl.BlockSpec((1, 9, dm), lambda i: (i // tps, 0, 0))
    row = _row_spec(tm, dm)
    stat_spec = pl.BlockSpec((1, SUBLANE, dm), lambda i: (i // tps, 0, 0))
    loss_spec = pl.BlockSpec((SUBLANE, LANE), lambda i: (0, 0))

    def body(x_ref, f_ref, t_ref, mod_ref, dy_ref, df_ref, st_ref, loss_ref):
        i = pl.program_id(0)
        g = mod_ref[0, 8:9, :]
        f = f_ref[...]
        err = x_ref[...] + 0.5 * g * f - t_ref[...]
        dy = err * (1.0 / dm)
        dy_ref[...] = dy
        df_ref[...] = (0.5 * g * dy).astype(df_ref.dtype)
        dgate = jnp.sum(0.5 * dy * f, axis=0, keepdims=True)
        part = 0.5 * jnp.sum(jnp.sum(err * err, axis=0, keepdims=True), axis=1, keepdims=True) * (1.0 / dm)

        @pl.when(i % tps == 0)
        def _():
            st_ref[...] = jnp.zeros_like(st_ref)

        @pl.when(i == 0)
        def _():
            loss_ref[...] = jnp.zeros_like(loss_ref)

        st_ref[0, 0:1, :] += dgate
        loss_ref[...] += jnp.broadcast_to(part, loss_ref.shape)

    return _pcall(body, name=name, grid=(T // tm,), in_specs=[row, row, row, mod_spec],
                  out_specs=[row, row, stat_spec, loss_spec],
                  out_shape=[_sds((T, dm), F32), _sds((T, dm), MX), _sds((mod3.shape[0], SUBLANE, dm), F32),
                             _sds((SUBLANE, LANE), F32)])(x2, f2, tgt, mod3)


def _bwd_block(x_cur, dh, dx_in, f_prev, mod3, gain, *, sub, coef, name, tm=256, comm=None):
    T, dm = x_cur.shape
    nb = mod3.shape[0]
    tps = (T // nb) // tm
    has_f = f_prev is not None
    mod_spec = pl.BlockSpec((1, 9, dm), lambda i: (i // tps, 0, 0))
    vec_spec = pl.BlockSpec((1, dm), lambda i: (0, 0))
    stat_spec = pl.BlockSpec((1, SUBLANE, dm), lambda i: (i // tps, 0, 0))
    row = _row_spec(tm, dm)

    def body(*refs):
        if has_f:
            x_ref, dh_ref, dxi_ref, f_ref, mod_ref, n_ref, dx_ref, df_ref, st_ref = refs
        else:
            x_ref, dh_ref, dxi_ref, mod_ref, n_ref, dx_ref, st_ref = refs
        i = pl.program_id(0)
        x = x_ref[...]
        r = _rsq(x)
        xhat = x * r
        n = n_ref[...]
        dhv = dh_ref[...]
        d_shift = jnp.sum(dhv, axis=0, keepdims=True)
        d_scale = jnp.sum(dhv * (xhat * n), axis=0, keepdims=True)
        dxn = dhv * (1.0 + mod_ref[0, 3 * sub + 1:3 * sub + 2, :])
        d_gain = jnp.sum(dxn * xhat, axis=0, keepdims=True)
        dxhat = dxn * n
        dx = dxi_ref[...] + r * (dxhat - xhat * jnp.mean(dxhat * xhat, axis=-1, keepdims=True))
        dx_ref[...] = dx

        @pl.when(i % tps == 0)
        def _():
            st_ref[...] = jnp.zeros_like(st_ref)

        st_ref[0, 0:1, :] += d_shift
        st_ref[0, 1:2, :] += d_scale
        st_ref[0, 2:3, :] += d_gain
        if has_f:
            f = f_ref[...]
            st_ref[0, 3:4, :] += jnp.sum(coef * dx * f, axis=0, keepdims=True)
            df_ref[...] = (coef * mod_ref[0, 3 * sub - 1:3 * sub, :] * dx).astype(df_ref.dtype)

    st_shape = _sds((nb, SUBLANE, dm), F32)
    if has_f:
        return _pcall(body, name=name, grid=(T // tm,), in_specs=[row, row, row, row, mod_spec, vec_spec],
                      out_specs=[row, row, stat_spec],
                      out_shape=[_sds((T, dm), F32), _sds((T, dm), MX), st_shape], comm=comm)(x_cur, dh, dx_in, f_prev, mod3, gain)
    return _pcall(body, name=name, grid=(T // tm,), in_specs=[row, row, row, mod_spec, vec_spec],
                  out_specs=[row, stat_spec], out_shape=[_sds((T, dm), F32), st_shape], comm=comm)(x_cur, dh, dx_in, mod3, gain)


def _merge_fwd(z, br_pool, br_mla, *, name, tm=256):
    T = z.shape[0]

    def body(z_ref, bp_ref, bm_ref, o_ref):
        gp = z_ref[:, Z_GP:Z_GP + D]
        gm = z_ref[:, Z_GM:Z_GM + D]
        o_ref[...] = (_sigmoid(gp) * bp_ref[...] + _sigmoid(gm) * bm_ref[...]).astype(o_ref.dtype)

    return _pcall(body, name=name, grid=(T // tm,), in_specs=[_row_spec(tm, Z_W), _row_spec(tm, D), _row_spec(tm, D)],
                  out_specs=_row_spec(tm, D), out_shape=_sds((T, D), MX))(z, br_pool, br_mla)


def _merge_bwd(z, br_pool, br_mla, dmerged, *, name, tm=256):
    T = z.shape[0]

    def body(z_ref, bp_ref, bm_ref, dm_ref, dbp_ref, dbm_ref, dg_ref):
        dm = dm_ref[...]
        sp = _sigmoid(z_ref[:, Z_GP:Z_GP + D])
        sm = _sigmoid(z_ref[:, Z_GM:Z_GM + D])
        dbp_ref[...] = (dm * sp).astype(dbp_ref.dtype)
        dbm_ref[...] = (dm * sm).astype(dbm_ref.dtype)
        dg_ref[:, :D] = (dm * bp_ref[...] * sp * (1.0 - sp)).astype(dg_ref.dtype)
        dg_ref[:, D:] = (dm * bm_ref[...] * sm * (1.0 - sm)).astype(dg_ref.dtype)

    return _pcall(body, name=name, grid=(T // tm,),
                  in_specs=[_row_spec(tm, Z_W), _row_spec(tm, D), _row_spec(tm, D), _row_spec(tm, D)],
                  out_specs=[_row_spec(tm, D), _row_spec(tm, D), _row_spec(tm, 2 * D)],
                  out_shape=[_sds((T, D), MX), _sds((T, D), MX), _sds((T, 2 * D), MX)])(z, br_pool, br_mla, dmerged)


def _shift_down(x, k, row):
    return jnp.where(row >= k, pltpu.roll(x, k, 0), 0.0)


def _shift_up(x, k, row, n):
    return jnp.where(row < n - k, pltpu.roll(x, n - k, 0), 0.0)


def _pool_fwd(z, pool_grp, pool_scale, *, nb, name):
    T = z.shape[0]
    S = T // nb
    blk = pl.BlockSpec((S, LANE), lambda b, g: (b, g))

    def body(u_ref, w_ref, s_ref, pooled_ref, mixed_ref, scaled_ref):
        g = pl.program_id(1)
        u = u_ref[...]
        row = lax.broadcasted_iota(jnp.int32, u.shape, 0)
        s2 = u + _shift_down(u, 1, row)
        s4 = s2 + _shift_down(s2, 2, row)
        s8 = s4 + _shift_down(s4, 4, row)
        s16 = s8 + _shift_down(s8, 8, row)
        win = jnp.where(g == 0, s2, jnp.where(g == 1, s4, jnp.where(g == 2, s8, s16)))
        width = lax.shift_left(jnp.int32(2), g)
        cnt = jnp.minimum(row + 1, width).astype(F32)
        pooled = (win / cnt - u).astype(MX)
        pooled_ref[...] = pooled
        mixed = jnp.dot(pooled, w_ref[0], preferred_element_type=F32)
        mixed_ref[...] = mixed
        scaled_ref[...] = (mixed * s_ref[...]).astype(scaled_ref.dtype)

    return _pcall(body, name=name, grid=(nb, POOL_G),
                  in_specs=[blk, pl.BlockSpec((1, LANE, LANE), lambda b, g: (g, 0, 0)),
                            pl.BlockSpec((1, LANE), lambda b, g: (0, g))],
                  out_specs=[blk, blk, blk],
                  out_shape=[_sds((T, POOL_W), MX), _sds((T, POOL_W), F32), _sds((T, POOL_W), MX)])(z, pool_grp, pool_scale)


def _pool_bwd(dscaled, mixed, pooled, pool_grp, pool_scale, *, nb, name):
    T = dscaled.shape[0]
    S = T // nb
    blk = pl.BlockSpec((S, LANE), lambda g, b: (b, g))

    def body(ds_ref, mixed_ref, pooled_ref, w_ref, s_ref, du_ref, dw_ref, dsc_ref):
        g = pl.program_id(0)
        b = pl.program_id(1)
        ds = ds_ref[...]
        dsc_ref[0] = jnp.sum(ds * mixed_ref[...], axis=0, keepdims=True)
        dmixed = (ds * s_ref[...]).astype(MX)
        dw = lax.dot_general(pooled_ref[...], dmixed, (((0,), (0,)), ((), ())), preferred_element_type=F32)

        @pl.when(b == 0)
        def _():
            dw_ref[0] = dw

        @pl.when(b > 0)
        def _():
            dw_ref[0] += dw
        dpooled = lax.dot_general(dmixed, w_ref[0], (((1,), (1,)), ((), ())), preferred_element_type=F32)
        row = lax.broadcasted_iota(jnp.int32, dpooled.shape, 0)
        width = lax.shift_left(jnp.int32(2), g)
        q = dpooled / jnp.minimum(row + 1, width).astype(F32)
        r2 = q + _shift_up(q, 1, row, S)
        r4 = r2 + _shift_up(r2, 2, row, S)
        r8 = r4 + _shift_up(r4, 4, row, S)
        r16 = r8 + _shift_up(r8, 8, row, S)
        win = jnp.where(g == 0, r2, jnp.where(g == 1, r4, jnp.where(g == 2, r8, r16)))
        du_ref[...] = (win - dpooled).astype(du_ref.dtype)

    return _pcall(body, name=name, grid=(POOL_G, nb),
                  in_specs=[blk, blk, blk, pl.BlockSpec((1, LANE, LANE), lambda g, b: (g, 0, 0)),
                            pl.BlockSpec((1, LANE), lambda g, b: (0, g))],
                  out_specs=[blk, pl.BlockSpec((1, LANE, LANE), lambda g, b: (g, 0, 0)),
                             pl.BlockSpec((1, 1, LANE), lambda g, b: (b, 0, g))],
                  out_shape=[_sds((T, POOL_W), MX), _sds((POOL_G, LANE, LANE), F32), _sds((nb, 1, POOL_W), F32)],
                  )(dscaled, mixed, pooled, pool_grp, pool_scale)


def _lat_fwd(z, q_gain, kv_gain, *, name, tm=256):
    T = z.shape[0]

    def body(z_ref, qg_ref, kg_ref, qn_ref, kvn_ref):
        ql = z_ref[:, Z_QL:Z_QL + QL]
        kv = z_ref[:, Z_KV:Z_KV + KVL]
        qn_ref[...] = (ql * _rsq(ql) * qg_ref[...]).astype(qn_ref.dtype)
        kvn_ref[...] = (kv * _rsq(kv) * kg_ref[...]).astype(kvn_ref.dtype)

    return _pcall(body, name=name, grid=(T // tm,),
                  in_specs=[_row_spec(tm, LAT_W), pl.BlockSpec((1, QL), lambda i: (0, 0)), pl.BlockSpec((1, KVL), lambda i: (0, 0))],
                  out_specs=[_row_spec(tm, QL), _row_spec(tm, KVL)],
                  out_shape=[_sds((T, QL), MX), _sds((T, KVL), MX)])(z, q_gain, kv_gain)


def _lat_bwd(z, dqn, dkvn, q_gain, kv_gain, *, nb, name, tm=256):
    T = z.shape[0]
    tps = (T // nb) // tm

    def norm_bwd(x, dy, gain):
        r = _rsq(x)
        xhat = x * r
        dgain = jnp.sum(dy * xhat, axis=0, keepdims=True)
        dxhat = dy * gain
        return r * (dxhat - xhat * jnp.mean(dxhat * xhat, axis=-1, keepdims=True)), dgain

    def body(z_ref, dq_ref, dkv_ref, qg_ref, kg_ref, dql_ref, dkvl_ref, sq_ref, sk_ref):
        i = pl.program_id(0)
        dql, dqg = norm_bwd(z_ref[:, Z_QL:Z_QL + QL], dq_ref[...], qg_ref[...])
        dkvl, dkg = norm_bwd(z_ref[:, Z_KV:Z_KV + KVL], dkv_ref[...], kg_ref[...])
        dql_ref[...] = dql.astype(dql_ref.dtype)
        dkvl_ref[...] = dkvl.astype(dkvl_ref.dtype)

        @pl.when(i % tps == 0)
        def _():
            sq_ref[...] = jnp.zeros_like(sq_ref)
            sk_ref[...] = jnp.zeros_like(sk_ref)

        sq_ref[0, 0:1, :] += dqg
        sk_ref[0, 0:1, :] += dkg

    return _pcall(body, name=name, grid=(T // tm,),
                  in_specs=[_row_spec(tm, LAT_W), _row_spec(tm, QL), _row_spec(tm, KVL),
                            pl.BlockSpec((1, QL), lambda i: (0, 0)), pl.BlockSpec((1, KVL), lambda i: (0, 0))],
                  out_specs=[_row_spec(tm, QL), _row_spec(tm, KVL),
                             pl.BlockSpec((1, SUBLANE, QL), lambda i: (i // tps, 0, 0)),
                             pl.BlockSpec((1, SUBLANE, KVL), lambda i: (i // tps, 0, 0))],
                  out_shape=[_sds((T, QL), MX), _sds((T, KVL), MX), _sds((nb, SUBLANE, QL), F32), _sds((nb, SUBLANE, KVL), F32)],
                  )(z, dqn, dkvn, q_gain, kv_gain)


def _lane_masks(shape):
    lane = lax.broadcasted_iota(jnp.int32, shape, len(shape) - 1)
    m_n = lane < NOPE
    m_r = jnp.logical_and(lane >= KR_LANE, lane < KR_LANE + ROPE)
    first_half = lane < KR_LANE + ROPE // 2
    return m_n, m_r, first_half


def _rot(y, first_half):
    return jnp.where(first_half, -pltpu.roll(y, LANE - ROPE // 2, 1), pltpu.roll(y, ROPE // 2, 1))


def _rot_t(v, first_half, m_r):
    return jnp.where(m_r, jnp.where(first_half, pltpu.roll(v, LANE - ROPE // 2, 1), -pltpu.roll(v, ROPE // 2, 1)), 0.0)


def _prep_fwd(qp, kvp, z, cos, sin, q_gain, k_gain, *, name, tm=256):
    T = qp.shape[0]
    slab = pl.BlockSpec((tm, LANE), lambda i: (i, 0))
    kr_spec = pl.BlockSpec((tm, LANE), lambda i: (i, Z_KR // LANE))
    vec = pl.BlockSpec((1, LANE), lambda i: (0, 0))

    def body(qp_ref, kvp_ref, kr_ref, cos_ref, sin_ref, qg_ref, kg_ref, q_ref, k_ref, v_ref):
        m_n, m_r, first_half = _lane_masks((tm, LANE))
        c = cos_ref[...]
        s = sin_ref[...]
        qg = qg_ref[...]
        kg = kg_ref[...]
        xr = kr_ref[...]
        rr = lax.rsqrt(jnp.sum(xr * xr, axis=-1, keepdims=True) * (1.0 / ROPE) + EPS)
        yr = xr * rr * kg
        kr = jnp.where(m_r, yr * c + _rot(yr, first_half) * s, 0.0)
        for h in range(NH):
            x = qp_ref[:, h * LANE:(h + 1) * LANE]
            x2 = x * x
            rn = lax.rsqrt(jnp.sum(jnp.where(m_n, x2, 0.0), axis=-1, keepdims=True) * (1.0 / NOPE) + EPS)
            rq = lax.rsqrt(jnp.sum(jnp.where(m_r, x2, 0.0), axis=-1, keepdims=True) * (1.0 / ROPE) + EPS)
            y = x * jnp.where(m_n, rn, jnp.where(m_r, rq, 0.0)) * qg
            q_ref[:, h * LANE:(h + 1) * LANE] = (y * c + _rot(y, first_half) * s).astype(q_ref.dtype)
            xk = kvp_ref[:, h * LANE:(h + 1) * LANE]
            rk = lax.rsqrt(jnp.sum(jnp.where(m_n, xk * xk, 0.0), axis=-1, keepdims=True) * (1.0 / NOPE) + EPS)
            k_ref[:, h * LANE:(h + 1) * LANE] = (jnp.where(m_n, xk * rk * kg, 0.0) + kr).astype(k_ref.dtype)
        v_ref[...] = kvp_ref[:, NH * LANE:].astype(v_ref.dtype)

    return _pcall(body, name=name, grid=(T // tm,),
                  in_specs=[_row_spec(tm, NH * LANE), _row_spec(tm, NH * LANE + NH * NOPE), kr_spec, slab, slab, vec, vec],
                  out_specs=[_row_spec(tm, NH * LANE), _row_spec(tm, NH * LANE), _row_spec(tm, NH * NOPE)],
                  out_shape=[_sds((T, NH * LANE), MX), _sds((T, NH * LANE), MX), _sds((T, NH * NOPE), MX)],
                  )(qp, kvp, z, cos, sin, q_gain, k_gain)


def _prep_bwd(dq, dk, dv, qp, kvp, z, cos, sin, q_gain, k_gain, *, nb, name, tm=256):
    T = qp.shape[0]
    tps = (T // nb) // tm
    slab = pl.BlockSpec((tm, LANE), lambda i: (i, 0))
    kr_spec = pl.BlockSpec((tm, LANE), lambda i: (i, Z_KR // LANE))
    vec = pl.BlockSpec((1, LANE), lambda i: (0, 0))

    def body(dq_ref, dk_ref, dv_ref, qp_ref, kvp_ref, kr_ref, cos_ref, sin_ref, qg_ref, kg_ref,
             dqp_ref, dkvp_ref, dkr_ref, st_ref):
        i = pl.program_id(0)
        m_n, m_r, first_half = _lane_masks((tm, LANE))
        c = cos_ref[...]
        s = sin_ref[...]
        qg = qg_ref[...]
        kg = kg_ref[...]
        dqg = jnp.zeros((1, LANE), F32)
        dkg = jnp.zeros((1, LANE), F32)
        dkr_sum = jnp.zeros((tm, LANE), F32)
        for h in range(NH):
            x = qp_ref[:, h * LANE:(h + 1) * LANE]
            x2 = x * x
            rn = lax.rsqrt(jnp.sum(jnp.where(m_n, x2, 0.0), axis=-1, keepdims=True) * (1.0 / NOPE) + EPS)
            rq = lax.rsqrt(jnp.sum(jnp.where(m_r, x2, 0.0), axis=-1, keepdims=True) * (1.0 / ROPE) + EPS)
            rfac = jnp.where(m_n, rn, jnp.where(m_r, rq, 0.0))
            xhat = x * rfac
            do = dq_ref[:, h * LANE:(h + 1) * LANE]
            dy = do * c + _rot_t(do * s, first_half, m_r)
            dqg = dqg + jnp.sum(dy * xhat, axis=0, keepdims=True)
            dxhat = dy * qg
            t = dxhat * xhat
            mean_n = jnp.sum(jnp.where(m_n, t, 0.0), axis=-1, keepdims=True) * (1.0 / NOPE)
            mean_r = jnp.sum(jnp.where(m_r, t, 0.0), axis=-1, keepdims=True) * (1.0 / ROPE)
            dqp_ref[:, h * LANE:(h + 1) * LANE] = (
                rfac * (dxhat - xhat * jnp.where(m_n, mean_n, jnp.where(m_r, mean_r, 0.0)))).astype(dqp_ref.dtype)

            xk = kvp_ref[:, h * LANE:(h + 1) * LANE]
            rk = lax.rsqrt(jnp.sum(jnp.where(m_n, xk * xk, 0.0), axis=-1, keepdims=True) * (1.0 / NOPE) + EPS)
            khat = jnp.where(m_n, xk * rk, 0.0)
            dko = dk_ref[:, h * LANE:(h + 1) * LANE]
            dkn = jnp.where(m_n, dko, 0.0)
            dkg = dkg + jnp.sum(dkn * khat, axis=0, keepdims=True)
            dkhat = dkn * kg
            mean_k = jnp.sum(dkhat * khat, axis=-1, keepdims=True) * (1.0 / NOPE)
            dkvp_ref[:, h * LANE:(h + 1) * LANE] = jnp.where(m_n, rk * (dkhat - khat * mean_k), 0.0).astype(dkvp_ref.dtype)
            dkr_sum = dkr_sum + jnp.where(m_r, dko, 0.0)
        dkvp_ref[:, NH * LANE:] = dv_ref[...].astype(dkvp_ref.dtype)

        xr = kr_ref[...]
        rr = lax.rsqrt(jnp.sum(xr * xr, axis=-1, keepdims=True) * (1.0 / ROPE) + EPS)
        rhat = xr * rr
        dyr = dkr_sum * c + _rot_t(dkr_sum * s, first_half, m_r)
        dkg = dkg + jnp.sum(dyr * rhat, axis=0, keepdims=True)
        drhat = dyr * kg
        mean_kr = jnp.sum(drhat * rhat, axis=-1, keepdims=True) * (1.0 / ROPE)
        dkr_ref[...] = jnp.where(m_r, rr * (drhat - rhat * mean_kr), 0.0).astype(dkr_ref.dtype)

        @pl.when(i % tps == 0)
        def _():
            st_ref[...] = jnp.zeros_like(st_ref)

        st_ref[0, 0:1, :] += dqg
        st_ref[0, 1:2, :] += dkg

    return _pcall(body, name=name, grid=(T // tm,),
                  in_specs=[_row_spec(tm, NH * LANE), _row_spec(tm, NH * LANE), _row_spec(tm, NH * NOPE),
                            _row_spec(tm, NH * LANE), _row_spec(tm, NH * LANE + NH * NOPE), kr_spec, slab, slab, vec, vec],
                  out_specs=[_row_spec(tm, NH * LANE), _row_spec(tm, NH * LANE + NH * NOPE), slab,
                             pl.BlockSpec((1, SUBLANE, LANE), lambda i: (i // tps, 0, 0))],
                  out_shape=[_sds((T, NH * LANE), MX), _sds((T, NH * LANE + NH * NOPE), MX), _sds((T, LANE), MX),
                             _sds((nb, SUBLANE, LANE), F32)],
                  )(dq, dk, dv, qp, kvp, z, cos, sin, q_gain, k_gain)


def _lower_triangle(t):
    return lax.broadcasted_iota(jnp.int32, (t, t), 1) <= lax.broadcasted_iota(jnp.int32, (t, t), 0)


def _attn_fwd(q, k, v, *, nb, name, tq=512, comm=None):
    T = q.shape[0]
    S = T // nb
    tq = _pick(S, tq)
    nq = S // tq
    tk = tq
    npair = NH // 2

    def body(q_ref, k_ref, v_ref, o_ref, lse_ref):
        qi = pl.program_id(2)
        lane = lax.broadcasted_iota(jnp.int32, (tq, LANE), 1)
        qs = [q_ref[:, hh * LANE:(hh + 1) * LANE] for hh in range(2)]

        def block(j, carry, diagonal):
            k0 = pl.multiple_of(j * tk, tk)
            vb = v_ref[pl.ds(k0, tk), :]
            new = []
            for hh in range(2):
                m, l, acc = carry[hh]
                kb = k_ref[pl.ds(k0, tk), hh * LANE:(hh + 1) * LANE]
                s = lax.dot_general(qs[hh], kb, (((1,), (1,)), ((), ())), preferred_element_type=F32) * ATTN_SCALE
                if diagonal:
                    s = jnp.where(_lower_triangle(tq), s, NEG)
                m_new = jnp.maximum(m, jnp.max(s, axis=-1, keepdims=True))
                p = jnp.exp(s - m_new)
                alpha = jnp.exp(m - m_new)
                l = alpha * l + jnp.sum(p, axis=-1, keepdims=True)
                acc = alpha * acc + jnp.dot(p.astype(MX), vb, preferred_element_type=F32)
                new.append((m_new, l, acc))
            return tuple(new)

        init = tuple((jnp.full((tq, 1), NEG, F32), jnp.zeros((tq, 1), F32), jnp.zeros((tq, LANE), F32)) for _ in range(2))
        carry = lax.fori_loop(0, qi, lambda j, c: block(j, c, False), init)
        (m0, l0, acc0), (m1, l1, acc1) = block(qi, carry, True)
        o_ref[...] = jnp.where(lane < NOPE, acc0 / l0, acc1 / l1).astype(o_ref.dtype)
        lse_ref[...] = jnp.where(lane < NOPE, m0 + jnp.log(l0), m1 + jnp.log(l1))

    return _pcall(body, name=name, grid=(nb, npair, nq),
                  in_specs=[pl.BlockSpec((tq, 2 * LANE), lambda b, p, i: (b * nq + i, p)),
                            pl.BlockSpec((S, 2 * LANE), lambda b, p, i: (b, p)),
                            pl.BlockSpec((S, LANE), lambda b, p, i: (b, p))],
                  out_specs=[pl.BlockSpec((tq, LANE), lambda b, p, i: (b * nq + i, p)),
                             pl.BlockSpec((tq, LANE), lambda b, p, i: (b * nq + i, p))],
                  out_shape=[_sds((T, NH * NOPE), MX), _sds((T, NH * NOPE), F32)], comm=comm)(q, k, v)


def _attn_bwd(q, k, v, o, lse, do, *, nb, name, tq=512, comm=None):
    T = q.shape[0]
    S = T // nb
    tq = _pick(S, tq)
    nq = S // tq
    tk = tq
    npair = NH // 2

    def body(q_ref, k_ref, v_ref, o_ref, lse_ref, do_ref, dq_ref, dk_ref, dv_ref, delta_ref):
        lane = lax.broadcasted_iota(jnp.int32, (tq, LANE), 1)
        first = lane < NOPE
        dq_ref[...] = jnp.zeros_like(dq_ref)

        def delta_step(qi, _):
            q0 = pl.multiple_of(qi * tq, tq)
            prod = do_ref[pl.ds(q0, tq), :] * o_ref[pl.ds(q0, tq), :].astype(F32)
            d0 = jnp.sum(jnp.where(first, prod, 0.0), axis=-1, keepdims=True)
            d1 = jnp.sum(jnp.where(first, 0.0, prod), axis=-1, keepdims=True)
            delta_ref[pl.ds(q0, tq), :] = jnp.where(first, d0, d1)
            return 0

        lax.fori_loop(0, nq, delta_step, 0)

        def kv_step(kj, _):
            k0 = pl.multiple_of(kj * tk, tk)
            kbs = [k_ref[pl.ds(k0, tk), hh * LANE:(hh + 1) * LANE] for hh in range(2)]
            vb = v_ref[pl.ds(k0, tk), :]

            def q_block(qi, carry, diagonal):
                dk0, dk1, dv = carry
                dks = [dk0, dk1]
                q0 = pl.multiple_of(qi * tq, tq)
                dov = do_ref[pl.ds(q0, tq), :]
                lse_v = lse_ref[pl.ds(q0, tq), :]
                delta_v = delta_ref[pl.ds(q0, tq), :]
                for hh in range(2):
                    qh = q_ref[pl.ds(q0, tq), hh * LANE:(hh + 1) * LANE]
                    dob = jnp.where(first if hh == 0 else jnp.logical_not(first), dov, 0.0).astype(MX)
                    s = lax.dot_general(qh, kbs[hh], (((1,), (1,)), ((), ())), preferred_element_type=F32) * ATTN_SCALE
                    p = jnp.exp(s - lse_v[:, hh * NOPE:hh * NOPE + 1])
                    if diagonal:
                        p = jnp.where(_lower_triangle(tq), p, 0.0)
                    dp = lax.dot_general(dob, vb, (((1,), (1,)), ((), ())), preferred_element_type=F32)
                    ds = (p * (dp - delta_v[:, hh * NOPE:hh * NOPE + 1]) * ATTN_SCALE).astype(MX)
                    dks[hh] = dks[hh] + lax.dot_general(ds, qh, (((0,), (0,)), ((), ())), preferred_element_type=F32)
                    dv = dv + lax.dot_general(p.astype(MX), dob, (((0,), (0,)), ((), ())), preferred_element_type=F32)
                    dq_ref[pl.ds(q0, tq), hh * LANE:(hh + 1) * LANE] += jnp.dot(ds, kbs[hh], preferred_element_type=F32)
                return dks[0], dks[1], dv

            zero = jnp.zeros((tk, LANE), F32)
            carry = q_block(kj, (zero, zero, zero), True)
            dk0, dk1, dv = lax.fori_loop(kj + 1, nq, lambda qi, c: q_block(qi, c, False), carry)
            dk_ref[pl.ds(k0, tk), 0:LANE] = dk0
            dk_ref[pl.ds(k0, tk), LANE:2 * LANE] = dk1
            dv_ref[pl.ds(k0, tk), :] = dv
            return 0

        lax.fori_loop(0, nq, kv_step, 0)

    pair256 = pl.BlockSpec((S, 2 * LANE), lambda b, p: (b, p))
    pair128 = pl.BlockSpec((S, LANE), lambda b, p: (b, p))
    return _pcall(body, name=name, grid=(nb, npair),
                  in_specs=[pair256, pair256, pair128, pair128, pair128, pair128],
                  out_specs=[pair256, pair256, pair128],
                  out_shape=[_sds((T, NH * LANE), F32), _sds((T, NH * LANE), F32), _sds((T, NH * NOPE), F32)],
                  scratch=[pltpu.VMEM((S, LANE), F32)], comm=comm)(q, k, v, o, lse, do)


class _NoExchange:
    def __init__(self, weights):
        self.W, self.G = weights, {}

    def rider(self, name):
        return None

    def landed(self, name, outs):
        pass

    def grad(self, key, g):
        self.G[key] = g


def _run(plan, fn, *args, name, **kw):
    rider = plan.rider(name)
    if rider is None:
        return fn(*args, name=name, **kw)
    outs, landed = fn(*args, name=name, comm=rider, **kw)
    plan.landed(name, landed)
    return outs


def _layer_fwd_bwd(x, tgt, mod3, cos, sin, plan, P):
    nb = mod3.shape[0]
    W = plan.W
    h1 = _fwd_block(x, None, mod3, P["norm_ffn1"], sub=0, coef=0.0, name="fwd_norm1")
    gu1, a1 = _run(plan, _ffn_up, h1, W["ffn1_in"], name="ffn1_up")
    f1 = _mm(a1, W["ffn1_out"], mode="nn", out_dtype=F32, name="ffn1_out", tm=1024, tn=1024, tk=DFF)
    x1, h2 = _fwd_block(x, f1, mod3, P["norm_mix"], sub=1, coef=0.5, name="fwd_norm2")
    z = _mm(h2, W["w_in"], mode="nn", out_dtype=F32, name="mix_in", tn=1664)
    pooled, mixed, scaled = _pool_fwd(z, W["pool_grp"], P["pool_scale"], nb=nb, name="pool_fwd")
    br_pool = _mm(scaled, W["pool_proj"], mode="nn", out_dtype=F32, name="pool_proj")
    qn, kvn = _lat_fwd(z, P["q_a_norm"], P["kv_a_norm"], name="lat_fwd")
    qp = _mm(qn, W["q_up"], mode="nn", out_dtype=F32, name="q_up")
    kvp = _mm(kvn, W["kv_up"], mode="nn", out_dtype=F32, name="kv_up")
    q, k, v = _prep_fwd(qp, kvp, z, cos, sin, P["q_gain"], P["k_gain"], name="prep_fwd")
    attn, lse = _run(plan, _attn_fwd, q, k, v, nb=nb, name="attn_fwd")
    br_mla = _mm(attn, W["mla_proj"], mode="nn", out_dtype=F32, name="mla_proj")
    merged = _merge_fwd(z, br_pool, br_mla, name="merge_fwd")
    mo = _mm(merged, W["w_out"], mode="nn", out_dtype=F32, name="mix_out")
    x2, h3 = _fwd_block(x1, mo, mod3, P["norm_ffn2"], sub=2, coef=1.0, name="fwd_norm3")
    gu2, a2 = _ffn_up(h3, W["ffn2_in"], name="ffn2_up")
    f2 = _mm(a2, W["ffn2_out"], mode="nn", out_dtype=F32, name="ffn2_out", tm=1024, tn=1024, tk=DFF)
    dy, df2, st_fin, loss = _final(x2, f2, tgt, mod3, name="loss_head")

    plan.grad("ffn2_out", _dw(a2, df2, name="d_ffn2_out"))
    dgu2 = _ffn_dact(df2, gu2, W["ffn2_out"], name="ffn2_dact")
    dh3 = _ffn_dh(dgu2, W["ffn2_in"], name="d_ffn2_h")
    plan.grad("ffn2_in", _ffn_dw_in(h3, dgu2, name="d_ffn2_in"))
    dx2, dmo, st3 = _run(plan, _bwd_block, x2, dh3, dy, mo, mod3, P["norm_ffn2"], sub=2, coef=1.0, name="bwd_norm3")
    plan.grad("w_out", _dw(merged, dmo, name="d_mix_out"))
    dmerged = _mm(dmo, W["w_out"], mode="nt", out_dtype=F32, name="d_merged")
    dbr_pool, dbr_mla, dgates = _merge_bwd(z, br_pool, br_mla, dmerged, name="merge_bwd")
    plan.grad("pool_proj", _dw(scaled, dbr_pool, name="d_pool_proj"))
    dscaled = _mm(dbr_pool, W["pool_proj"], mode="nt", out_dtype=F32, name="d_pool_scaled")
    du_pool, d_pool_grp, d_pool_scale = _pool_bwd(dscaled, mixed, pooled, W["pool_grp"], P["pool_scale"], nb=nb, name="pool_bwd")
    plan.grad("mla_proj", _dw(attn, dbr_mla, name="d_mla_proj"))
    dattn = _mm(dbr_mla, W["mla_proj"], mode="nt", out_dtype=F32, name="d_attn")
    dq, dk, dv = _run(plan, _attn_bwd, q, k, v, attn, lse, dattn, nb=nb, name="attn_bwd")
    dqp, dkvp, dkr, st_prep = _prep_bwd(dq, dk, dv, qp, kvp, z, cos, sin, P["q_gain"], P["k_gain"], nb=nb, name="prep_bwd")
    plan.grad("q_up", _dw(qn, dqp, name="d_q_up"))
    dqn = _mm(dqp, W["q_up"], mode="nt", out_dtype=F32, name="d_qn")
    plan.grad("kv_up", _dw(kvn, dkvp, name="d_kv_up"))
    dkvn = _mm(dkvp, W["kv_up"], mode="nt", out_dtype=F32, name="d_kvn")
    dql, dkvl, st_q, st_kv = _lat_bwd(z, dqn, dkvn, P["q_a_norm"], P["kv_a_norm"], nb=nb, name="lat_bwd")
    dz = jnp.concatenate([du_pool, dql, dkvl, dkr, dgates], axis=1)
    plan.grad("w_in", _dw(h2, dz, name="d_mix_in", tn=1664))
    dh2 = _run(plan, _mm, dz, W["w_in"], mode="nt", out_dtype=F32, name="d_mix_h")
    dx1, df1, st2 = _bwd_block(x1, dh2, dx2, f1, mod3, P["norm_mix"], sub=1, coef=0.5, name="bwd_norm2")
    plan.grad("ffn1_out", _dw(a1, df1, name="d_ffn1_out"))
    dgu1 = _run(plan, _ffn_dact, df1, gu1, W["ffn1_out"], name="ffn1_dact")
    dh1 = _run(plan, _ffn_dh, dgu1, W["ffn1_in"], name="d_ffn1_h")
    plan.grad("ffn1_in", _run(plan, _ffn_dw_in, h1, dgu1, name="d_ffn1_in"))
    grad_x, st1 = _run(plan, _bwd_block, x, dh1, dx1, None, mod3, P["norm_ffn1"], sub=0, coef=0.0, name="bwd_norm1")

    return loss, grad_x, (st1, st2, st3, st_fin, st_q, st_kv, st_prep, d_pool_scale), d_pool_grp


def _w_in_to_kernel(w):
    k = w.shape[0]
    zeros = lambda n: jnp.zeros((k, n), w.dtype)
    return jnp.concatenate([w[:, 0:1152], zeros(KR_LANE), w[:, 1152:1184], zeros(LANE - KR_LANE - ROPE), w[:, 1184:]], axis=1)


def _w_in_from_kernel(g):
    return jnp.concatenate([g[:, 0:1152], g[:, Z_KR + KR_LANE:Z_KR + KR_LANE + ROPE], g[:, Z_GP:]], axis=1)


def _q_up_to_kernel(w):
    k = w.shape[0]
    return jnp.pad(w.reshape(k, NH, NOPE + ROPE), ((0, 0), (0, 0), (0, LANE - NOPE - ROPE))).reshape(k, NH * LANE)


def _q_up_from_kernel(g):
    k = g.shape[0]
    return g.reshape(k, NH, LANE)[:, :, :NOPE + ROPE].reshape(k, NH * (NOPE + ROPE))


def _kv_up_to_kernel(w):
    k = w.shape[0]
    w3 = w.reshape(k, NH, 2 * NOPE)
    kpart = jnp.pad(w3[:, :, :NOPE], ((0, 0), (0, 0), (0, LANE - NOPE))).reshape(k, NH * LANE)
    return jnp.concatenate([kpart, w3[:, :, NOPE:].reshape(k, NH * NOPE)], axis=1)


def _kv_up_from_kernel(g):
    k = g.shape[0]
    kpart = g[:, :NH * LANE].reshape(k, NH, LANE)[:, :, :NOPE]
    vpart = g[:, NH * LANE:].reshape(k, NH, NOPE)
    return jnp.concatenate([kpart, vpart], axis=2).reshape(k, NH * 2 * NOPE)


def _gain_slab(nope, rope):
    return jnp.concatenate([nope, rope, jnp.zeros((1, LANE - NOPE - ROPE), nope.dtype)], axis=1)


def _rope_tables(positions):
    inv_freq = 10000.0 ** (-jnp.arange(0, ROPE, 2, dtype=F32) / ROPE)
    ang = positions.astype(F32)[:, None] * inv_freq
    ang = jnp.concatenate([ang, ang], axis=-1)
    t = positions.shape[0]
    cos = jnp.concatenate([jnp.ones((t, KR_LANE), F32), jnp.cos(ang), jnp.ones((t, LANE - KR_LANE - ROPE), F32)], axis=1)
    sin = jnp.concatenate([jnp.zeros((t, KR_LANE), F32), jnp.sin(ang), jnp.zeros((t, LANE - KR_LANE - ROPE), F32)], axis=1)
    return cos, sin


def _coords():
    return lax.axis_index("x"), lax.axis_index("y"), lax.axis_index("c")


HBM_SPEC = pl.BlockSpec(memory_space=pl.ANY)
VMEM_SPEC = pl.BlockSpec(memory_space=pltpu.VMEM)


def _gather8(v, *, name, comm=None):
    r, c = v.shape

    def body(v_ref, out_ref, send_sems, recv_sems, local_sem):
        x, y, cc = _coords()
        me = 4 * x + 2 * y + cc
        mine = pltpu.make_async_copy(v_ref, out_ref.at[me], local_sem)
        mine.start()
        copies = []
        for kk in range(1, N_DEV):
            peer = (x ^ (kk >> 2), y ^ ((kk >> 1) & 1), cc ^ (kk & 1))
            cp = pltpu.make_async_remote_copy(src_ref=v_ref, dst_ref=out_ref.at[me], send_sem=send_sems.at[kk - 1],
                                              recv_sem=recv_sems.at[kk - 1], device_id=peer, device_id_type=MESH)
            cp.start()
            copies.append(cp)
        for kk in range(1, N_DEV):
            peer_slot = me ^ kk
            pltpu.make_async_remote_copy(src_ref=v_ref, dst_ref=out_ref.at[peer_slot], send_sem=send_sems.at[kk - 1],
                                         recv_sem=recv_sems.at[kk - 1], device_id=(x, y, cc), device_id_type=MESH).wait_recv()
        for cp in copies:
            cp.wait_send()
        mine.wait()

    return _pcall(body, name=name, out_shape=_sds((N_DEV, r, c), v.dtype), in_specs=[VMEM_SPEC], out_specs=VMEM_SPEC,
                  scratch=[pltpu.SemaphoreType.DMA((N_DEV - 1,)), pltpu.SemaphoreType.DMA((N_DEV - 1,)), pltpu.SemaphoreType.DMA],
                  comm=comm)(v)


CHIP_RELS = ((1, 0), (0, 1), (1, 1))


def _comm_only(comm, *, name):
    _, outs = _pcall(lambda: None, name=name, out_shape=[], in_specs=[], out_specs=[], comm=comm)()
    return outs


def _gather_comm(shards):
    n = len(shards)

    def ici(ins, outs, sems, a, j, x, y, cc):
        half = ins[a].shape[0] // 2
        mine = pl.ds(cc * half, half)
        dx, dy = CHIP_RELS[j]
        return pltpu.make_async_remote_copy(src_ref=ins[a].at[mine], dst_ref=outs[a].at[2 * x + y, mine],
                                            send_sem=sems[0].at[a, j], recv_sem=sems[1].at[a, j],
                                            device_id=(x ^ dx, y ^ dy, cc), device_id_type=MESH)

    def d2d(ins, outs, sems, a, j, x, y, cc, half_of):
        half = ins[a].shape[0] // 2
        dx, dy = CHIP_RELS[j]
        landed = outs[a].at[2 * (x ^ dx) + (y ^ dy), pl.ds(half_of * half, half)]
        return pltpu.make_async_remote_copy(src_ref=landed, dst_ref=landed, send_sem=sems[2].at[a, j], recv_sem=sems[3].at[a, j],
                                            device_id=(x, y, 1 - cc), device_id_type=MESH)

    def own(ins, outs, sems, a, x, y):
        return pltpu.make_async_copy(ins[a], outs[a].at[2 * x + y], sems[4].at[a])

    def start(ins, outs, sems):
        x, y, cc = _coords()
        for a in range(n):
            own(ins, outs, sems, a, x, y).start()
            for j in range(3):
                ici(ins, outs, sems, a, j, x, y, cc).start()

    def finish(ins, outs, sems):
        x, y, cc = _coords()
        for a in range(n):
            for j in range(3):
                ici(ins, outs, sems, a, j, x, y, cc).wait_recv()
                d2d(ins, outs, sems, a, j, x, y, cc, cc).start()
        for a in range(n):
            for j in range(3):
                d2d(ins, outs, sems, a, j, x, y, cc, 1 - cc).wait_recv()
        for a in range(n):
            for j in range(3):
                ici(ins, outs, sems, a, j, x, y, cc).wait_send()
                d2d(ins, outs, sems, a, j, x, y, cc, cc).wait_send()
            own(ins, outs, sems, a, x, y).wait()

    dma = pltpu.SemaphoreType.DMA
    return _Comm(shards, [_sds((N_CHIP,) + s.shape, s.dtype) for s in shards],
                 [dma((n, 3)), dma((n, 3)), dma((n, 3)), dma((n, 3)), dma((n,))], start, finish)


def _swap_comm(parts):
    n = len(parts)

    def copy(ins, outs, sems, a):
        x, y, cc = _coords()
        half = ins[a].shape[1] // 2
        return pltpu.make_async_remote_copy(src_ref=ins[a].at[:, pl.ds((1 - cc) * half, half)], dst_ref=outs[a],
                                            send_sem=sems[0].at[a], recv_sem=sems[1].at[a], device_id=(x, y, 1 - cc),
                                            device_id_type=MESH)

    def start(ins, outs, sems):
        for a in range(n):
            copy(ins, outs, sems, a).start()

    def finish(ins, outs, sems):
        for a in range(n):
            copy(ins, outs, sems, a).wait()

    dma = pltpu.SemaphoreType.DMA
    return _Comm(parts, [_sds((p.shape[0], p.shape[1] // 2, p.shape[2]), p.dtype) for p in parts], [dma((n,)), dma((n,))],
                 start, finish)


def _add_half(full, other, cidx, *, name):
    nch, r, c = full.shape
    half = r // 2
    tr = _pick_rows(half)
    nbk = half // tr
    grid_spec = pltpu.PrefetchScalarGridSpec(
        num_scalar_prefetch=1, grid=(nch, nbk),
        in_specs=[pl.BlockSpec((1, tr, c), lambda j, i, cref: (j, cref[0] * nbk + i, 0)),
                  pl.BlockSpec((1, tr, c), lambda j, i, cref: (j, i, 0))],
        out_specs=pl.BlockSpec((1, tr, c), lambda j, i, cref: (j, i, 0)))

    def body(cref, a_ref, b_ref, o_ref):
        o_ref[...] = (a_ref[...] + b_ref[...]).astype(o_ref.dtype)

    return _pcall(body, name=name, out_shape=_sds((nch, half, c), MX), grid_spec=grid_spec)(cidx, full, other)


def _pick_rows(rows, target=512):
    best = None
    for t in range(16, min(rows, target) + 1, 16):
        if rows % t == 0:
            best = t
    return rows if best is None else best


def _exchange_comm(parts):
    n = len(parts)

    def send(ins, outs, sems, a, j, x, y, cc):
        dx, dy = CHIP_RELS[j]
        return pltpu.make_async_remote_copy(src_ref=ins[a].at[2 * (x ^ dx) + (y ^ dy)], dst_ref=outs[a].at[2 * x + y],
                                            send_sem=sems[0].at[a, j], recv_sem=sems[1].at[a, j],
                                            device_id=(x ^ dx, y ^ dy, cc), device_id_type=MESH)

    def landing(ins, outs, sems, a, j, x, y, cc):
        dx, dy = CHIP_RELS[j]
        peer_chip = 2 * (x ^ dx) + (y ^ dy)
        return pltpu.make_async_remote_copy(src_ref=ins[a].at[peer_chip], dst_ref=outs[a].at[peer_chip], send_sem=sems[0].at[a, j],
                                            recv_sem=sems[1].at[a, j], device_id=(x, y, cc), device_id_type=MESH)

    def own(ins, outs, sems, a, x, y):
        return pltpu.make_async_copy(ins[a].at[2 * x + y], outs[a].at[2 * x + y], sems[2].at[a])

    def start(ins, outs, sems):
        x, y, cc = _coords()
        for a in range(n):
            own(ins, outs, sems, a, x, y).start()
            for j in range(3):
                send(ins, outs, sems, a, j, x, y, cc).start()

    def finish(ins, outs, sems):
        x, y, cc = _coords()
        for a in range(n):
            for j in range(3):
                landing(ins, outs, sems, a, j, x, y, cc).wait_recv()
        for a in range(n):
            for j in range(3):
                send(ins, outs, sems, a, j, x, y, cc).wait_send()
            own(ins, outs, sems, a, x, y).wait()

    dma = pltpu.SemaphoreType.DMA
    return _Comm(parts, [_sds(p.shape, p.dtype) for p in parts], [dma((n, 3)), dma((n, 3)), dma((n,))], start, finish)


def _sum_chips(q, cidx, *, name):
    nch, h, c = q.shape
    tr = _pick_rows(h)
    nbk = h // tr
    grid_spec = pltpu.PrefetchScalarGridSpec(
        num_scalar_prefetch=1, grid=(nbk,),
        in_specs=[pl.BlockSpec((nch, tr, c), lambda i, cref: (0, i, 0))],
        out_specs=pl.BlockSpec((tr, c), lambda i, cref: (cref[0] * nbk + i, 0)))

    def body(cref, q_ref, o_ref):
        acc = q_ref[0].astype(F32) + q_ref[1].astype(F32)
        acc = acc + q_ref[2].astype(F32)
        o_ref[...] = acc + q_ref[3].astype(F32)

    return _pcall(body, name=name, out_shape=_sds((2 * h, c), F32), grid_spec=grid_spec)(cidx, q)


def _join_comm(fulls):
    n = len(fulls)

    def half_copy(outs, sems, a, which):
        x, y, cc = _coords()
        h = outs[a].shape[0] // 2
        rows = outs[a].at[pl.ds((cc if which == 0 else 1 - cc) * h, h)]
        return pltpu.make_async_remote_copy(src_ref=rows, dst_ref=rows, send_sem=sems[0].at[a], recv_sem=sems[1].at[a],
                                            device_id=(x, y, 1 - cc), device_id_type=MESH)

    def start(ins, outs, sems):
        for a in range(n):
            half_copy(outs, sems, a, 0).start()

    def finish(ins, outs, sems):
        for a in range(n):
            half_copy(outs, sems, a, 1).wait_recv()
        for a in range(n):
            half_copy(outs, sems, a, 0).wait_send()

    dma = pltpu.SemaphoreType.DMA
    return _Comm(fulls, [_sds(p.shape, p.dtype) for p in fulls], [dma((n,)), dma((n,))], start, finish,
                 aliases={a: a for a in range(n)})


def _ada_fwd(c_all, w, b, *, name):
    m, kdim = c_all.shape
    n = w.shape[1]
    tn = _pick(n, 1152)

    def body(c_ref, w_ref, b_ref, o_ref):
        cv = c_ref[...]
        act = (cv * _sigmoid(cv)).astype(MX)
        o_ref[...] = jnp.dot(act, w_ref[...].astype(MX), preferred_element_type=F32) + b_ref[...]

    return _pcall(body, name=name, out_shape=_sds((m, n), F32), grid=(n // tn,),
                  in_specs=[pl.BlockSpec((m, kdim), lambda j: (0, 0)), pl.BlockSpec((kdim, tn), lambda j: (0, j)),
                            pl.BlockSpec((1, tn), lambda j: (0, j))],
                  out_specs=pl.BlockSpec((m, tn), lambda j: (0, j)))(c_all, w, b)


def _ada_bwd(c_all, dmod_cols, *, name):
    m, kdim = c_all.shape
    n = dmod_cols.shape[1]
    tn = _pick(n, 1152)

    def body(c_ref, d_ref, o_ref):
        cv = c_ref[...]
        act = (cv * _sigmoid(cv)).astype(MX)
        o_ref[...] = lax.dot_general(act, d_ref[...].astype(MX), (((0,), (0,)), ((), ())), preferred_element_type=F32)

    return _pcall(body, name=name, out_shape=_sds((kdim, n), F32), grid=(n // tn,),
                  in_specs=[pl.BlockSpec((m, kdim), lambda j: (0, 0)), pl.BlockSpec((m, tn), lambda j: (0, j))],
                  out_specs=pl.BlockSpec((kdim, tn), lambda j: (0, j)))(c_all, dmod_cols)


SLAB_W = 1024
RED_ROWS = 8


def _pack_stats(st1, st2, st3, st_fin, st_q, st_kv, st_prep, d_pool_scale, *, name):
    nb = st1.shape[0]
    dm_rows = -(-9 * nb // SUBLANE) * SUBLANE

    def body(s1, s2, s3, sf, sq, skv, sp, sps, o_ref):
        o_ref[...] = jnp.zeros_like(o_ref)
        for s in range(nb):
            rows = [s1[s, 0:1, :], s1[s, 1:2, :], s2[s, 3:4, :], s2[s, 0:1, :], s2[s, 1:2, :], s3[s, 3:4, :], s3[s, 0:1, :],
                    s3[s, 1:2, :], sf[s, 0:1, :]]
            for k, row in enumerate(rows):
                o_ref[9 * s + k:9 * s + k + 1, :] = row

        def over_seq(ref, r):
            acc = ref[0, r:r + 1, :]
            for s in range(1, nb):
                acc = acc + ref[s, r:r + 1, :]
            return acc

        o_ref[dm_rows + 0:dm_rows + 1, :] = over_seq(s1, 2)
        o_ref[dm_rows + 1:dm_rows + 2, :] = over_seq(s2, 2)
        o_ref[dm_rows + 2:dm_rows + 3, :] = over_seq(s3, 2)
        o_ref[dm_rows + 3:dm_rows + 4, 0:POOL_W] = over_seq(sps, 0)
        o_ref[dm_rows + 4:dm_rows + 5, 0:QL] = over_seq(sq, 0)
        o_ref[dm_rows + 5:dm_rows + 6, 0:KVL] = over_seq(skv, 0)
        o_ref[dm_rows + 6:dm_rows + 7, 0:LANE] = over_seq(sp, 0)
        o_ref[dm_rows + 7:dm_rows + 8, 0:LANE] = over_seq(sp, 1)

    return _pcall(body, name=name, out_shape=_sds((dm_rows + RED_ROWS, SLAB_W), F32), in_specs=[VMEM_SPEC] * 8,
                  out_specs=VMEM_SPEC)(st1, st2, st3, st_fin, st_q, st_kv, st_prep, d_pool_scale)


def _small_allreduce(slab, pool, *, name, comm=None):
    rows, w = slab.shape
    dm = rows - RED_ROWS
    prow, pw = pool.shape
    crow = RED_ROWS + 2 * dm

    def body(slab_ref, pool_ref, red_ref, dmod_ref, ptot_ref, sib_slab, sib_pool, chip_slab, chip_pool, land_slab, land_pool,
             a_send, a_recv, b_send, b_recv):
        x, y, cc = _coords()
        chip = 2 * x + y
        sib = (x, y, 1 - cc)
        to_sib = [pltpu.make_async_remote_copy(src_ref=slab_ref, dst_ref=sib_slab, send_sem=a_send.at[0], recv_sem=a_recv.at[0],
                                               device_id=sib, device_id_type=MESH),
                  pltpu.make_async_remote_copy(src_ref=pool_ref, dst_ref=sib_pool, send_sem=a_send.at[1], recv_sem=a_recv.at[1],
                                               device_id=sib, device_id_type=MESH)]
        for cp in to_sib:
            cp.start()
        for cp in to_sib:
            cp.wait()
        mine_dm, theirs_dm = slab_ref[0:dm, :], sib_slab[0:dm, :]
        chip_slab[0:RED_ROWS, :] = slab_ref[dm:, :] + sib_slab[dm:, :]
        chip_slab[RED_ROWS:RED_ROWS + dm, :] = jnp.where(cc == 0, mine_dm, theirs_dm)
        chip_slab[RED_ROWS + dm:, :] = jnp.where(cc == 0, theirs_dm, mine_dm)
        chip_pool[...] = pool_ref[...] + sib_pool[...]

        sends = []
        for j, (dx, dy) in enumerate(CHIP_RELS):
            peer = (x ^ dx, y ^ dy, cc)
            sends.append(pltpu.make_async_remote_copy(src_ref=chip_slab, dst_ref=land_slab.at[chip], send_sem=b_send.at[j, 0],
                                                      recv_sem=b_recv.at[j, 0], device_id=peer, device_id_type=MESH))
            sends.append(pltpu.make_async_remote_copy(src_ref=chip_pool, dst_ref=land_pool.at[chip], send_sem=b_send.at[j, 1],
                                                      recv_sem=b_recv.at[j, 1], device_id=peer, device_id_type=MESH))
        for cp in sends:
            cp.start()
        land_slab[chip] = chip_slab[...]
        land_pool[chip] = chip_pool[...]
        for j, (dx, dy) in enumerate(CHIP_RELS):
            peer_chip = 2 * (x ^ dx) + (y ^ dy)
            pltpu.make_async_remote_copy(src_ref=chip_slab, dst_ref=land_slab.at[peer_chip], send_sem=b_send.at[j, 0],
                                         recv_sem=b_recv.at[j, 0], device_id=(x, y, cc), device_id_type=MESH).wait_recv()
            pltpu.make_async_remote_copy(src_ref=chip_pool, dst_ref=land_pool.at[peer_chip], send_sem=b_send.at[j, 1],
                                         recv_sem=b_recv.at[j, 1], device_id=(x, y, cc), device_id_type=MESH).wait_recv()
        red = land_slab[0, 0:RED_ROWS, :]
        ptot = land_pool[0]
        for ch in range(1, N_CHIP):
            red = red + land_slab[ch, 0:RED_ROWS, :]
            ptot = ptot + land_pool[ch]
        red_ref[...] = red
        ptot_ref[...] = ptot
        for ch in range(N_CHIP):
            dmod_ref[2 * dm * ch:2 * dm * (ch + 1), :] = land_slab[ch, RED_ROWS:, :]
        for cp in sends:
            cp.wait_send()

    dma = pltpu.SemaphoreType.DMA
    return _pcall(body, name=name, in_specs=[VMEM_SPEC, VMEM_SPEC], out_specs=[VMEM_SPEC] * 3,
                  out_shape=[_sds((RED_ROWS, w), F32), _sds((N_DEV * dm, w), F32), _sds((prow, pw), F32)],
                  scratch=[pltpu.VMEM((rows, w), F32), pltpu.VMEM((prow, pw), F32), pltpu.VMEM((crow, w), F32),
                           pltpu.VMEM((prow, pw), F32), pltpu.VMEM((N_CHIP, crow, w), F32), pltpu.VMEM((N_CHIP, prow, pw), F32),
                           dma((2,)), dma((2,)), dma((3, 2)), dma((3, 2))], comm=comm)(slab, pool)


def _adamw_math(w, g, m, v):
    mn = ADAM_B1 * m + (1.0 - ADAM_B1) * g
    vn = ADAM_B2 * v + (1.0 - ADAM_B2) * (g * g)
    bc1 = 1.0 / (1.0 - ADAM_B1 ** ADAM_STEP)
    bc2 = 1.0 / (1.0 - ADAM_B2 ** ADAM_STEP)
    return -ADAM_LR * ((mn * bc1) / (jnp.sqrt(vn * bc2) + ADAM_EPS) + ADAM_WD * w), mn, vn


def _small_update(red, dmod_all, pool_total, nb, params, *, name):
    names = list(SMALL)
    dm = dmod_all.shape[0] // N_DEV

    def grad_of(nm, red_ref, dmod_ref, ptot_ref):
        if nm == "b_ada":
            acc = None
            for d in range(N_DEV):
                for s in range(nb):
                    blk = dmod_ref[d * dm + 9 * s:d * dm + 9 * s + 9, :]
                    acc = blk if acc is None else acc + blk
            return jnp.concatenate([acc[k:k + 1, :] for k in range(9)], axis=1)
        if nm == "pool_grp":
            return ptot_ref[...]
        row, lo, n = {"norm_ffn1": (0, 0, D), "norm_mix": (1, 0, D), "norm_ffn2": (2, 0, D), "pool_scale": (3, 0, POOL_W),
                      "q_a_norm": (4, 0, QL), "kv_a_norm": (5, 0, KVL), "q_norm_nope": (6, 0, NOPE),
                      "q_norm_rope": (6, NOPE, ROPE), "k_norm_nope": (7, 0, NOPE), "k_norm_rope": (7, KR_LANE, ROPE)}[nm]
        return red_ref[row:row + 1, lo:lo + n]

    def body(*refs):
        red_ref, dmod_ref, ptot_ref = refs[:3]
        ins = refs[3:3 + 3 * len(names)]
        outs = refs[3 + 3 * len(names):]
        for i, nm in enumerate(names):
            g = grad_of(nm, red_ref, dmod_ref, ptot_ref)
            d, mn, vn = _adamw_math(ins[3 * i][...], g, ins[3 * i + 1][...], ins[3 * i + 2][...])
            outs[4 * i][...] = g
            outs[4 * i + 1][...] = d
            outs[4 * i + 2][...] = mn
            outs[4 * i + 3][...] = vn

    flat_in = [a for nm in names for a in params[nm]]
    out_shape = [_sds(params[nm][0].shape, F32) for nm in names for _ in range(4)]
    res = _pcall(body, name=name, in_specs=[VMEM_SPEC] * (3 + len(flat_in)), out_specs=[VMEM_SPEC] * len(out_shape),
                 out_shape=out_shape)(red, dmod_all, pool_total, *flat_in)
    return {nm: tuple(res[4 * i:4 * i + 4]) for i, nm in enumerate(names)}


def _adamw(w, g, m, v, *, name, comm=None):
    r, c = w.shape
    tr = _pick_rows(r, 256)
    tc = c if tr < r else _pick(c, 256)
    spec = pl.BlockSpec((tr, tc), lambda i, j: (i, j))
    bc1 = 1.0 / (1.0 - ADAM_B1 ** ADAM_STEP)
    bc2 = 1.0 / (1.0 - ADAM_B2 ** ADAM_STEP)

    def body(w_ref, g_ref, m_ref, v_ref, d_ref, mo_ref, vo_ref):
        gv = g_ref[...]
        mn = ADAM_B1 * m_ref[...] + (1.0 - ADAM_B1) * gv
        vn = ADAM_B2 * v_ref[...] + (1.0 - ADAM_B2) * (gv * gv)
        mo_ref[...] = mn
        vo_ref[...] = vn
        d_ref[...] = -ADAM_LR * ((mn * bc1) / (jnp.sqrt(vn * bc2) + ADAM_EPS) + ADAM_WD * w_ref[...])

    out = _sds((r, c), F32)
    return _pcall(body, name=name, out_shape=[out, out, out], grid=(r // tr, c // tc), in_specs=[spec] * 4, out_specs=[spec] * 3,
                  comm=comm)(w, g, m, v)


BIG = ("w_ffn1_in", "w_ffn1_out", "w_in", "w_pool_proj", "w_q_up", "w_kv_up", "w_mla_proj", "w_out", "w_ffn2_in", "w_ffn2_out")
ROW_SHARDED = ("w_ffn1_out", "w_out", "w_ffn2_out")
KERNEL_NAME = {"w_ffn1_in": "ffn1_in", "w_ffn1_out": "ffn1_out", "w_in": "w_in", "w_pool_proj": "pool_proj", "w_q_up": "q_up",
               "w_kv_up": "kv_up", "w_mla_proj": "mla_proj", "w_out": "w_out", "w_ffn2_in": "ffn2_in", "w_ffn2_out": "ffn2_out"}
WEIGHTS = ("w_ada", "b_ada", "norm_ffn1", "w_ffn1_in", "w_ffn1_out", "norm_mix", "w_in", "pool_grp", "pool_scale", "w_pool_proj",
           "q_a_norm", "w_q_up", "kv_a_norm", "w_kv_up", "q_norm_nope", "q_norm_rope", "k_norm_nope", "k_norm_rope", "w_mla_proj",
           "w_out", "norm_ffn2", "w_ffn2_in", "w_ffn2_out")
SMALL = ("b_ada", "norm_ffn1", "norm_mix", "pool_grp", "pool_scale", "q_a_norm", "kv_a_norm", "q_norm_nope", "q_norm_rope",
         "k_norm_nope", "k_norm_rope", "norm_ffn2")
SMALL_SEQ = tuple(n for n in SMALL if n != "pool_grp")
SLAB_W = 1024


def _assemble(name, stacked):
    if name in ROW_SHARDED:
        return stacked.reshape(stacked.shape[0] * stacked.shape[1], stacked.shape[2])
    return jnp.transpose(stacked, (1, 0, 2)).reshape(stacked.shape[1], stacked.shape[0] * stacked.shape[2])


def _split(name, full):
    if name in ROW_SHARDED:
        return full.reshape(N_CHIP, full.shape[0] // N_CHIP, full.shape[1])
    return jnp.transpose(full.reshape(full.shape[0], N_CHIP, full.shape[1] // N_CHIP), (1, 0, 2))


GU = ("w_ffn1_in", "w_ffn2_in")
NARROW = ("w_in", "w_q_up")


def _to_rows(v, width=SLAB_W):
    flat = v.reshape(-1)
    rows = -(-flat.shape[0] // width)
    return jnp.pad(flat, (0, rows * width - flat.shape[0])).reshape(rows, width)


def _pack_small(parts, names=SMALL):
    rows, spans, at = [], {}, 0
    for name in names:
        r = _to_rows(parts[name])
        spans[name] = (at, parts[name].size, parts[name].shape)
        rows.append(r)
        at += r.shape[0]
    pad = (-at) % SUBLANE
    if pad:
        rows.append(jnp.zeros((pad, SLAB_W), F32))
    return jnp.concatenate(rows, axis=0), spans


def _unpack_small(slab, spans):
    out = {}
    for name, (at, size, shape) in spans.items():
        nrow = -(-size // SLAB_W)
        out[name] = slab[at:at + nrow].reshape(-1)[:size].reshape(shape)
    return out


GATHER_ON = {"gather_ffn1": ("w_ffn1_in", "w_ffn1_out"),
             "ffn1_up": ("w_in", "w_pool_proj", "w_q_up", "w_kv_up", "w_mla_proj", "w_out"),
             "attn_fwd": ("w_ffn2_in", "w_ffn2_out")}
SWAP_ON = {"bwd_norm3": ("w_ffn2_out", "w_ffn2_in"),
           "d_mix_h": ("w_out", "w_pool_proj", "w_mla_proj", "w_q_up", "w_kv_up", "w_in"),
           "ffn1_dact": ("w_ffn1_out",),
           "bwd_norm1": ("w_ffn1_in",)}
EXCHANGE_ON = {"attn_bwd": "bwd_norm3", "d_ffn1_h": "d_mix_h", "d_ffn1_in": "ffn1_dact", "small_allreduce": "bwd_norm1"}


class _ExchangePlan:
    def __init__(self, shards, cidx):
        self.shards, self.cidx = shards, cidx
        self.W, self.G, self.parts, self.pre, self.reduced = {}, {}, {}, {}, {}

    def grad(self, key, g):
        self.G[key] = g

    def rider(self, name):
        if name in GATHER_ON:
            return _gather_comm([self.shards[n] for n in GATHER_ON[name]])
        if name in SWAP_ON:
            self.parts[name] = [self._stacked(n) for n in SWAP_ON[name]]
            return _swap_comm(self.parts[name])
        if name in EXCHANGE_ON:
            return _exchange_comm(self.pre[EXCHANGE_ON[name]])
        return None

    def landed(self, name, outs):
        if name in GATHER_ON:
            for n, g in zip(GATHER_ON[name], outs):
                self.W[KERNEL_NAME[n]] = self._to_kernel(n, g)
        elif name in SWAP_ON:
            self.pre[name] = [_add_half(p, o, self.cidx, name="rs_add_" + KERNEL_NAME[n])
                              for n, p, o in zip(SWAP_ON[name], self.parts[name], outs)]
        elif name in EXCHANGE_ON:
            for n, q in zip(SWAP_ON[EXCHANGE_ON[name]], outs):
                self.reduced[n] = _sum_chips(q, self.cidx, name="rs_sum_" + KERNEL_NAME[n])

    @staticmethod
    def _to_kernel(n, stacked):
        if n in GU:
            return stacked
        full = _assemble(n, stacked)
        return {"w_in": _w_in_to_kernel, "w_q_up": _q_up_to_kernel, "w_kv_up": _kv_up_to_kernel}.get(n, lambda w: w)(full)

    def _stacked(self, n):
        g = self.G[KERNEL_NAME[n]]
        if n in GU:
            return g
        full = {"w_in": _w_in_from_kernel, "w_q_up": _q_up_from_kernel, "w_kv_up": _kv_up_from_kernel}.get(n, lambda w: w)(g)
        return _split(n, full)


def kernel(x, c, positions, w_ada, b_ada, norm_ffn1, w_ffn1_in, w_ffn1_out, norm_mix, w_in, pool_grp, pool_scale, w_pool_proj, q_a_norm, w_q_up, kv_a_norm, w_kv_up, q_norm_nope, q_norm_rope, k_norm_nope, k_norm_rope, w_mla_proj, w_out, norm_ffn2, w_ffn2_in, w_ffn2_out, loss_target, m_w_ada, m_b_ada, m_norm_ffn1, m_w_ffn1_in, m_w_ffn1_out, m_norm_mix, m_w_in, m_pool_grp, m_pool_scale, m_w_pool_proj, m_q_a_norm, m_w_q_up, m_kv_a_norm, m_w_kv_up, m_q_norm_nope, m_q_norm_rope, m_k_norm_nope, m_k_norm_rope, m_w_mla_proj, m_w_out, m_norm_ffn2, m_w_ffn2_in, m_w_ffn2_out, v_w_ada, v_b_ada, v_norm_ffn1, v_w_ffn1_in, v_w_ffn1_out, v_norm_mix, v_w_in, v_pool_grp, v_pool_scale, v_w_pool_proj, v_q_a_norm, v_w_q_up, v_kv_a_norm, v_w_kv_up, v_q_norm_nope, v_q_norm_rope, v_k_norm_nope, v_k_norm_rope, v_w_mla_proj, v_w_out, v_norm_ffn2, v_w_ffn2_in, v_w_ffn2_out):
    args = dict(locals())
    wts = {n: args[n][0] for n in WEIGHTS}
    mom = {n: args["m_" + n][0] for n in WEIGHTS}
    var = {n: args["v_" + n][0] for n in WEIGHTS}
    nb, seq, dm = x.shape
    tokens = nb * seq
    xi, yi, ci = _coords()
    chip = 2 * xi + yi
    dev = 4 * xi + 2 * yi + ci

    c_all = _gather8(c, name="gather_c").reshape(N_DEV * nb, dm)
    ncol = w_ada.shape[2]
    b_cols = lax.dynamic_slice_in_dim(wts["b_ada"].reshape(1, -1), chip * ncol, ncol, axis=1)
    mod_cols = _ada_fwd(c_all, wts["w_ada"], b_cols, name="ada_fwd")
    mod_all = _gather8(mod_cols, name="gather_mod")
    mine = lax.dynamic_slice_in_dim(mod_all, dev * nb, nb, axis=1)
    mod = jnp.concatenate([mine[0], mine[2], mine[4], mine[6]], axis=1)
    mod3 = mod.reshape(nb, 9, dm)

    cidx = ci.astype(jnp.int32).reshape(1)
    plan = _ExchangePlan({n: wts[n].astype(MX) for n in BIG}, cidx)
    plan.W["pool_grp"] = wts["pool_grp"].astype(MX)
    plan.landed("gather_ffn1", _comm_only(plan.rider("gather_ffn1"), name="gather_ffn1"))
    P = {"norm_ffn1": wts["norm_ffn1"].reshape(1, dm), "norm_mix": wts["norm_mix"].reshape(1, dm),
         "norm_ffn2": wts["norm_ffn2"].reshape(1, dm), "pool_scale": wts["pool_scale"].reshape(1, POOL_W),
         "q_a_norm": wts["q_a_norm"].reshape(1, QL), "kv_a_norm": wts["kv_a_norm"].reshape(1, KVL),
         "q_gain": _gain_slab(wts["q_norm_nope"].reshape(1, NOPE), wts["q_norm_rope"].reshape(1, ROPE)),
         "k_gain": _gain_slab(wts["k_norm_nope"].reshape(1, NOPE), wts["k_norm_rope"].reshape(1, ROPE))}
    cos, sin = _rope_tables(positions.reshape(tokens))

    loss8, grad_x, stats, d_pool_grp = _layer_fwd_bwd(x.reshape(tokens, dm), loss_target.reshape(tokens, dm), mod3, cos, sin, plan, P)
    loss = lax.psum(loss8[0, 0], ("x", "y", "c"))


    slab = _pack_stats(*stats, name="pack_stats")
    red, dmod_rows, pool_total = _run(plan, _small_allreduce, slab, d_pool_grp.reshape(POOL_G * LANE, LANE), name="small_allreduce")
    joined = _comm_only(_join_comm([plan.reduced[n] for n in BIG]), name="rs_join_halves")
    grads = {n: g for n, g in zip(BIG, joined)}
    dm_rows = dmod_rows.shape[0] // N_DEV
    dmod_all = dmod_rows.reshape(N_DEV, dm_rows, SLAB_W)[:, :9 * nb].reshape(N_DEV * nb, 9 * dm)
    dmod_cols = lax.dynamic_slice_in_dim(dmod_all, chip * ncol, ncol, axis=1)
    grads["w_ada"] = _ada_bwd(c_all, dmod_cols, name="ada_bwd")

    delta, new_m, new_v = {}, {}, {}
    as2d = lambda a: a.reshape(POOL_G * LANE, LANE) if a.ndim == 4 else a.reshape(1, -1)
    upd = _small_update(red, dmod_rows, pool_total, nb, {n: tuple(as2d(args[p + n]) for p in ("", "m_", "v_")) for n in SMALL},
                        name="small_update")
    for n in SMALL:
        grads[n], delta[n], new_m[n], new_v[n] = upd[n]
    for n in ("w_ada",) + BIG:
        if n in NARROW:
            res = _adamw(wts[n].T, grads[n].T, mom[n].T, var[n].T, name="adamw_" + n)
            delta[n], new_m[n], new_v[n] = (r.T for r in res)
        else:
            delta[n], new_m[n], new_v[n] = _adamw(wts[n], grads[n], mom[n], var[n], name="adamw_" + n)

    def lead(a, n):
        return a.reshape((1,) + wts[n].shape)

    return (loss, grad_x.reshape(nb, seq, dm), *[lead(grads[n], n) for n in WEIGHTS], *[lead(delta[n], n) for n in WEIGHTS],
            *[lead(new_m[n], n) for n in WEIGHTS], *[lead(new_v[n], n) for n in WEIGHTS])
```

```python
import functools
import math

import jax
import jax.numpy as jnp
from jax import lax
from jax.experimental import pallas as pl
from jax.experimental.pallas import tpu as pltpu

F32 = jnp.float32
MX = jnp.bfloat16

D = 1024
DFF = 2816
NH = 8
POOL_W = 512
POOL_G = 4
QL = 384
KVL = 256
ROPE = 32
NOPE = 64
LANE = 128
SUBLANE = 8
EPS = 1e-6
ATTN_SCALE = 1.0 / math.sqrt(96.0)
NEG = -1e30

Z_UP, Z_QL, Z_KV, Z_KR, Z_GP, Z_GM, Z_W = 0, 512, 896, 1152, 1280, 2304, 3328
KR_LANE = 64
LAT_W = 1280

ADAM_LR, ADAM_B1, ADAM_B2, ADAM_EPS, ADAM_WD, ADAM_STEP = 0.001, 0.9, 0.999, 1e-08, 0.01, 10

VMEM_LIMIT = 48 * 1024 * 1024
MESH = pl.DeviceIdType.MESH
N_DEV = 8
N_CHIP = 4


class _Comm:
    def __init__(self, ins, out_shapes, sems, start, finish, aliases=None):
        self.ins, self.out_shapes, self.sems = list(ins), list(out_shapes), list(sems)
        self.start, self.finish = start, finish
        self.aliases = aliases or {}


def _pcall(body, *, name, out_shape, grid=(), in_specs=None, out_specs=None, scratch=(), grid_spec=None, aliases=None,
           comm=None):
    params = pltpu.CompilerParams(vmem_limit_bytes=VMEM_LIMIT)
    kw = dict(name=name, compiler_params=params)
    if comm is None:
        if aliases:
            kw["input_output_aliases"] = aliases
        if grid_spec is not None:
            return pl.pallas_call(body, grid_spec=grid_spec, out_shape=out_shape, **kw)
        return pl.pallas_call(body, grid=grid, in_specs=in_specs, out_specs=out_specs, scratch_shapes=scratch,
                              out_shape=out_shape, **kw)
    single = not isinstance(out_shape, (list, tuple))
    outs = [out_shape] if single else list(out_shape)
    ospecs = [out_specs] if single else list(out_specs)
    n_in, n_out, n_ci, n_co, n_scr = len(in_specs), len(outs), len(comm.ins), len(comm.out_shapes), len(scratch)
    io = dict(aliases or {})
    io.update({n_in + i: n_out + o for i, o in comm.aliases.items()})

    def riding(*refs):
        ins, cins = refs[:n_in], refs[n_in:n_in + n_ci]
        at = n_in + n_ci
        os_, couts = refs[at:at + n_out], refs[at + n_out:at + n_out + n_co]
        at += n_out + n_co
        scr, csems = refs[at:at + n_scr], refs[at + n_scr:]
        if grid:
            first = functools.reduce(jnp.logical_and, [pl.program_id(d) == 0 for d in range(len(grid))])
            last = functools.reduce(jnp.logical_and, [pl.program_id(d) == grid[d] - 1 for d in range(len(grid))])
            pl.when(first)(lambda: comm.start(cins, couts, csems))
            body(*ins, *os_, *scr)
            pl.when(last)(lambda: comm.finish(cins, couts, csems))
        else:
            comm.start(cins, couts, csems)
            body(*ins, *os_, *scr)
            comm.finish(cins, couts, csems)

    call = pl.pallas_call(riding, grid=grid, in_specs=list(in_specs) + [HBM_SPEC] * n_ci, out_specs=ospecs + [HBM_SPEC] * n_co,
                          out_shape=outs + comm.out_shapes, scratch_shapes=list(scratch) + comm.sems,
                          input_output_aliases=io, **kw)

    def run(*args):
        res = call(*args, *comm.ins)
        main = list(res[:n_out])
        return (main[0] if single else main), list(res[n_out:])

    return run


def _pick(dim, target):
    best = None
    for t in range(LANE, min(dim, target) + 1, LANE):
        if dim % t == 0:
            best = t
    return dim if best is None else best


def _sds(shape, dtype):
    return jax.ShapeDtypeStruct(shape, dtype)


def _dw(a, g, *, name, tn=1024):
    return _mm(a, g, mode="tn", out_dtype=F32, name=name, tm=256, tn=tn, tk=a.shape[0], n_outer=True)


def _mm(a, b, *, mode, out_dtype, name, tm=1024, tn=1024, tk=4096, n_outer=False, comm=None):
    if mode == "nn":
        (M, K), (K2, N) = a.shape, b.shape
    elif mode == "nt":
        (M, K), (N, K2) = a.shape, b.shape
    else:
        (K, M), (K2, N) = a.shape, b.shape
    assert K == K2, (name, a.shape, b.shape)
    tm, tn, tk = _pick(M, tm), _pick(N, tn), _pick(K, tk)
    nk = K // tk
    if n_outer:
        ij = lambda g0, g1: (g1, g0)
        grid = (N // tn, M // tm, nk)
    else:
        ij = lambda g0, g1: (g0, g1)
        grid = (M // tm, N // tn, nk)
    if mode == "tn":
        a_spec = pl.BlockSpec((tk, tm), lambda g0, g1, k: (k, ij(g0, g1)[0]))
    else:
        a_spec = pl.BlockSpec((tm, tk), lambda g0, g1, k: (ij(g0, g1)[0], k))
    if mode == "nt":
        b_spec = pl.BlockSpec((tn, tk), lambda g0, g1, k: (ij(g0, g1)[1], k))
    else:
        b_spec = pl.BlockSpec((tk, tn), lambda g0, g1, k: (k, ij(g0, g1)[1]))
    o_spec = pl.BlockSpec((tm, tn), lambda g0, g1, k: ij(g0, g1))
    dn = {"nn": (((1,), (0,)), ((), ())), "nt": (((1,), (1,)), ((), ())), "tn": (((0,), (0,)), ((), ()))}[mode]

    def dot(a_ref, b_ref):
        return lax.dot_general(a_ref[...].astype(MX), b_ref[...].astype(MX), dn, preferred_element_type=F32)

    def body_one(a_ref, b_ref, o_ref):
        o_ref[...] = dot(a_ref, b_ref).astype(o_ref.dtype)

    def body_acc(a_ref, b_ref, o_ref, acc_ref):
        k = pl.program_id(2)
        part = dot(a_ref, b_ref)

        @pl.when(k == 0)
        def _():
            acc_ref[...] = part

        @pl.when(k > 0)
        def _():
            acc_ref[...] += part

        @pl.when(k == nk - 1)
        def _():
            o_ref[...] = acc_ref[...].astype(o_ref.dtype)

    return _pcall(body_one if nk == 1 else body_acc, name=name, out_shape=_sds((M, N), out_dtype), grid=grid,
                  in_specs=[a_spec, b_spec], out_specs=o_spec, scratch=[] if nk == 1 else [pltpu.VMEM((tm, tn), F32)],
                  comm=comm)(a, b)


def _gu_shard(q):
    return (q % 2) * 2 + q // 2


def _ffn_up(h, w_st, *, name, tm=512, comm=None):
    T, dm = h.shape
    hw = w_st.shape[2]
    tm = _pick(T, tm)

    def body(h_ref, wg_ref, wu_ref, gu_ref, a_ref):
        hv = h_ref[...]
        g = jnp.dot(hv, wg_ref[0], preferred_element_type=F32)
        u = jnp.dot(hv, wu_ref[0], preferred_element_type=F32)
        gu_ref[:, :hw] = g.astype(gu_ref.dtype)
        gu_ref[:, hw:] = u.astype(gu_ref.dtype)
        a_ref[...] = (g * _sigmoid(g) * u).astype(a_ref.dtype)

    return _pcall(body, name=name, grid=(T // tm, 2),
                  in_specs=[pl.BlockSpec((tm, dm), lambda i, j: (i, 0)), pl.BlockSpec((1, dm, hw), lambda i, j: (j, 0, 0)),
                            pl.BlockSpec((1, dm, hw), lambda i, j: (2 + j, 0, 0))],
                  out_specs=[pl.BlockSpec((tm, 2 * hw), lambda i, j: (i, j)), pl.BlockSpec((tm, hw), lambda i, j: (i, j))],
                  out_shape=[_sds((T, 4 * hw), MX), _sds((T, 2 * hw), MX)], comm=comm)(h, w_st, w_st)


def _ffn_up_normed(x, mod3, gain, w_st, *, sub, name, tm=512, comm=None):
    T, dm = x.shape
    hw = w_st.shape[2]
    tm = _pick(T, tm)
    tps = (T // mod3.shape[0]) // tm

    def body(x_ref, mod_ref, n_ref, wg_ref, wu_ref, h_ref, gu_ref, a_ref):
        @pl.when(pl.program_id(1) == 0)
        def _():
            xv = x_ref[...]
            xn = xv * _rsq(xv) * n_ref[...]
            h_ref[...] = (xn * (1.0 + mod_ref[0, 3 * sub + 1:3 * sub + 2, :]) + mod_ref[0, 3 * sub:3 * sub + 1, :]).astype(h_ref.dtype)

        hv = h_ref[...]
        g = jnp.dot(hv, wg_ref[0], preferred_element_type=F32)
        u = jnp.dot(hv, wu_ref[0], preferred_element_type=F32)
        gu_ref[:, :hw] = g.astype(gu_ref.dtype)
        gu_ref[:, hw:] = u.astype(gu_ref.dtype)
        a_ref[...] = (g * _sigmoid(g) * u).astype(a_ref.dtype)

    return _pcall(body, name=name, grid=(T // tm, 2),
                  in_specs=[pl.BlockSpec((tm, dm), lambda i, j: (i, 0)), pl.BlockSpec((1, 9, dm), lambda i, j: (i // tps, 0, 0)),
                            pl.BlockSpec((1, dm), lambda i, j: (0, 0)), pl.BlockSpec((1, dm, hw), lambda i, j: (j, 0, 0)),
                            pl.BlockSpec((1, dm, hw), lambda i, j: (2 + j, 0, 0))],
                  out_specs=[pl.BlockSpec((tm, dm), lambda i, j: (i, 0)), pl.BlockSpec((tm, 2 * hw), lambda i, j: (i, j)),
                             pl.BlockSpec((tm, hw), lambda i, j: (i, j))],
                  out_shape=[_sds((T, dm), MX), _sds((T, 4 * hw), MX), _sds((T, 2 * hw), MX)],
                  comm=comm)(x, mod3, gain, w_st, w_st)


def _ffn_dact(df, gu, w_out, *, name, tm=512, comm=None):
    T, dm = df.shape
    hw = gu.shape[1] // 4
    tm = _pick(T, tm)

    def body(df_ref, gu_ref, wo_ref, dgu_ref):
        da = lax.dot_general(df_ref[...], wo_ref[...], (((1,), (1,)), ((), ())), preferred_element_type=F32)
        g = gu_ref[:, :hw].astype(F32)
        u = gu_ref[:, hw:].astype(F32)
        s = _sigmoid(g)
        dgu_ref[:, :hw] = (da * u * (s * (1.0 + g * (1.0 - s)))).astype(dgu_ref.dtype)
        dgu_ref[:, hw:] = (da * (g * s)).astype(dgu_ref.dtype)

    return _pcall(body, name=name, grid=(T // tm, 2),
                  in_specs=[pl.BlockSpec((tm, dm), lambda i, j: (i, 0)), pl.BlockSpec((tm, 2 * hw), lambda i, j: (i, j)),
                            pl.BlockSpec((hw, dm), lambda i, j: (j, 0))],
                  out_specs=pl.BlockSpec((tm, 2 * hw), lambda i, j: (i, j)), out_shape=_sds(gu.shape, MX),
                  comm=comm)(df, gu, w_out)


def _ffn_dh(dgu, w_st, *, name, tm=1024, comm=None):
    T = dgu.shape[0]
    _, dm, hw = w_st.shape
    tm = _pick(T, tm)

    def body(d_ref, w_ref, o_ref, acc_ref):
        q = pl.program_id(1)
        part = lax.dot_general(d_ref[...], w_ref[0], (((1,), (1,)), ((), ())), preferred_element_type=F32)

        @pl.when(q == 0)
        def _():
            acc_ref[...] = part

        @pl.when(q > 0)
        def _():
            acc_ref[...] += part

        @pl.when(q == 3)
        def _():
            o_ref[...] = acc_ref[...]

    return _pcall(body, name=name, grid=(T // tm, 4),
                  in_specs=[pl.BlockSpec((tm, hw), lambda i, q: (i, q)), pl.BlockSpec((1, dm, hw), lambda i, q: (_gu_shard(q), 0, 0))],
                  out_specs=pl.BlockSpec((tm, dm), lambda i, q: (i, 0)), out_shape=_sds((T, dm), F32),
                  scratch=[pltpu.VMEM((tm, dm), F32)], comm=comm)(dgu, w_st)


def _ffn_dw_in(h, dgu, *, name, tm=256, comm=None):
    T, dm = h.shape
    hw = dgu.shape[1] // 4
    tm = _pick(dm, tm)

    def body(h_ref, d_ref, o_ref):
        o_ref[0] = lax.dot_general(h_ref[...], d_ref[...], (((0,), (0,)), ((), ())), preferred_element_type=F32)

    return _pcall(body, name=name, grid=(4, dm // tm),
                  in_specs=[pl.BlockSpec((T, tm), lambda q, i: (0, i)), pl.BlockSpec((T, hw), lambda q, i: (0, q))],
                  out_specs=pl.BlockSpec((1, tm, hw), lambda q, i: (_gu_shard(q), i, 0)),
                  out_shape=_sds((4, dm, hw), F32), comm=comm)(h, dgu)


def _rsq(x):
    return lax.rsqrt(jnp.mean(x * x, axis=-1, keepdims=True) + EPS)


def _sigmoid(x):
    return 1.0 / (1.0 + jnp.exp(-x))


def _row_spec(tm, w):
    return pl.BlockSpec((tm, w), lambda i: (i, 0))


def _fwd_block(x_prev, f_prev, mod3, gain, *, sub, coef, name, tm=256):
    T, dm = x_prev.shape
    tps = (T // mod3.shape[0]) // tm
    has_f = f_prev is not None
    mod_spec = pl.BlockSpec((1, 9, dm), lambda i: (i // tps, 0, 0))
    vec_spec = pl.BlockSpec((1, dm), lambda i: (0, 0))

    def body(*refs):
        if has_f:
            x_ref, f_ref, mod_ref, n_ref, xo_ref, h_ref = refs
            x = x_ref[...] + coef * mod_ref[0, 3 * sub - 1:3 * sub, :] * f_ref[...]
            xo_ref[...] = x
        else:
            x_ref, mod_ref, n_ref, h_ref = refs
            x = x_ref[...]
        xn = x * _rsq(x) * n_ref[...]
        h = xn * (1.0 + mod_ref[0, 3 * sub + 1:3 * sub + 2, :]) + mod_ref[0, 3 * sub:3 * sub + 1, :]
        h_ref[...] = h.astype(h_ref.dtype)

    row = _row_spec(tm, dm)
    if has_f:
        return _pcall(body, name=name, grid=(T // tm,), in_specs=[row, row, mod_spec, vec_spec], out_specs=[row, row],
                      out_shape=[_sds((T, dm), F32), _sds((T, dm), MX)])(x_prev, f_prev, mod3, gain)
    return _pcall(body, name=name, grid=(T // tm,), in_specs=[row, mod_spec, vec_spec], out_specs=row,
                  out_shape=_sds((T, dm), MX))(x_prev, mod3, gain)


def _final(x2, f2, tgt, mod3, *, name, tm=256):
    T, dm = x2.shape
    tps = (T // mod3.shape[0]) // tm
    mod_spec = pl.BlockSpec((1, 9, dm), lambda i: (i // tps, 0, 0))
    row = _row_spec(tm, dm)
    stat_spec = pl.BlockSpec((1, SUBLANE, dm), lambda i: (i // tps, 0, 0))
    loss_spec = pl.BlockSpec((SUBLANE, LANE), lambda i: (0, 0))

    def body(x_ref, f_ref, t_ref, mod_ref, dy_ref, df_ref, st_ref, loss_ref):
        i = pl.program_id(0)
        g = mod_ref[0, 8:9, :]
        f = f_ref[...]
        err = x_ref[...] + 0.5 * g * f - t_ref[...]
        dy = err * (1.0 / dm)
        dy_ref[...] = dy
        df_ref[...] = (0.5 * g * dy).astype(df_ref.dtype)
        dgate = jnp.sum(0.5 * dy * f, axis=0, keepdims=True)
        part = 0.5 * jnp.sum(jnp.sum(err * err, axis=0, keepdims=True), axis=1, keepdims=True) * (1.0 / dm)

        @pl.when(i % tps == 0)
        def _():
            st_ref[...] = jnp.zeros_like(st_ref)

        @pl.when(i == 0)
        def _():
            loss_ref[...] = jnp.zeros_like(loss_ref)

        st_ref[0, 0:1, :] += dgate
        loss_ref[...] += jnp.broadcast_to(part, loss_ref.shape)

    return _pcall(body, name=name, grid=(T // tm,), in_specs=[row, row, row, mod_spec],
                  out_specs=[row, row, stat_spec, loss_spec],
                  out_shape=[_sds((T, dm), F32), _sds((T, dm), MX), _sds((mod3.shape[0], SUBLANE, dm), F32),
                             _sds((SUBLANE, LANE), F32)])(x2, f2, tgt, mod3)


def _bwd_block(x_cur, dh, dx_in, f_prev, mod3, gain, *, sub, coef, name, tm=256, comm=None):
    T, dm = x_cur.shape
    nb = mod3.shape[0]
    tps = (T // nb) // tm
    has_f = f_prev is not None
    mod_spec = pl.BlockSpec((1, 9, dm), lambda i: (i // tps, 0, 0))
    vec_spec = pl.BlockSpec((1, dm), lambda i: (0, 0))
    stat_spec = pl.BlockSpec((1, SUBLANE, dm), lambda i: (i // tps, 0, 0))
    row = _row_spec(tm, dm)

    def body(*refs):
        if has_f:
            x_ref, dh_ref, dxi_ref, f_ref, mod_ref, n_ref, dx_ref, df_ref, st_ref = refs
        else:
            x_ref, dh_ref, dxi_ref, mod_ref, n_ref, dx_ref, st_ref = refs
        i = pl.program_id(0)
        x = x_ref[...]
        r = _rsq(x)
        xhat = x * r
        n = n_ref[...]
        dhv = dh_ref[...]
        d_shift = jnp.sum(dhv, axis=0, keepdims=True)
        d_scale = jnp.sum(dhv * (xhat * n), axis=0, keepdims=True)
        dxn = dhv * (1.0 + mod_ref[0, 3 * sub + 1:3 * sub + 2, :])
        d_gain = jnp.sum(dxn * xhat, axis=0, keepdims=True)
        dxhat = dxn * n
        dx = dxi_ref[...] + r * (dxhat - xhat * jnp.mean(dxhat * xhat, axis=-1, keepdims=True))
        dx_ref[...] = dx

        @pl.when(i % tps == 0)
        def _():
            st_ref[...] = jnp.zeros_like(st_ref)

        st_ref[0, 0:1, :] += d_shift
        st_ref[0, 1:2, :] += d_scale
        st_ref[0, 2:3, :] += d_gain
        if has_f:
            f = f_ref[...]
            st_ref[0, 3:4, :] += jnp.sum(coef * dx * f, axis=0, keepdims=True)
            df_ref[...] = (coef * mod_ref[0, 3 * sub - 1:3 * sub, :] * dx).astype(df_ref.dtype)

    st_shape = _sds((nb, SUBLANE, dm), F32)
    if has_f:
        return _pcall(body, name=name, grid=(T // tm,), in_specs=[row, row, row, row, mod_spec, vec_spec],
                      out_specs=[row, row, stat_spec],
                      out_shape=[_sds((T, dm), F32), _sds((T, dm), MX), st_shape], comm=comm)(x_cur, dh, dx_in, f_prev, mod3, gain)
    return _pcall(body, name=name, grid=(T // tm,), in_specs=[row, row, row, mod_spec, vec_spec],
                  out_specs=[row, stat_spec], out_shape=[_sds((T, dm), F32), st_shape], comm=comm)(x_cur, dh, dx_in, mod3, gain)


def _merge_fwd(z, br_pool, br_mla, *, name, tm=256):
    T = z.shape[0]

    def body(z_ref, bp_ref, bm_ref, o_ref):
        gp = z_ref[:, Z_GP:Z_GP + D]
        gm = z_ref[:, Z_GM:Z_GM + D]
        o_ref[...] = (_sigmoid(gp) * bp_ref[...] + _sigmoid(gm) * bm_ref[...]).astype(o_ref.dtype)

    return _pcall(body, name=name, grid=(T // tm,), in_specs=[_row_spec(tm, Z_W), _row_spec(tm, D), _row_spec(tm, D)],
                  out_specs=_row_spec(tm, D), out_shape=_sds((T, D), MX))(z, br_pool, br_mla)


def _merge_bwd(z, br_pool, br_mla, dmerged, *, name, tm=256):
    T = z.shape[0]

    def body(z_ref, bp_ref, bm_ref, dm_ref, dbp_ref, dbm_ref, dg_ref):
        dm = dm_ref[...]
        sp = _sigmoid(z_ref[:, Z_GP:Z_GP + D])
        sm = _sigmoid(z_ref[:, Z_GM:Z_GM + D])
        dbp_ref[...] = (dm * sp).astype(dbp_ref.dtype)
        dbm_ref[...] = (dm * sm).astype(dbm_ref.dtype)
        dg_ref[:, :D] = (dm * bp_ref[...] * sp * (1.0 - sp)).astype(dg_ref.dtype)
        dg_ref[:, D:] = (dm * bm_ref[...] * sm * (1.0 - sm)).astype(dg_ref.dtype)

    return _pcall(body, name=name, grid=(T // tm,),
                  in_specs=[_row_spec(tm, Z_W), _row_spec(tm, D), _row_spec(tm, D), _row_spec(tm, D)],
                  out_specs=[_row_spec(tm, D), _row_spec(tm, D), _row_spec(tm, 2 * D)],
                  out_shape=[_sds((T, D), MX), _sds((T, D), MX), _sds((T, 2 * D), MX)])(z, br_pool, br_mla, dmerged)


def _shift_down(x, k, row):
    return jnp.where(row >= k, pltpu.roll(x, k, 0), 0.0)


def _shift_up(x, k, row, n):
    return jnp.where(row < n - k, pltpu.roll(x, n - k, 0), 0.0)


def _pool_fwd(z, pool_grp, pool_scale, *, nb, name):
    T = z.shape[0]
    S = T // nb
    blk = pl.BlockSpec((S, LANE), lambda b, g: (b, g))

    def body(u_ref, w_ref, s_ref, pooled_ref, mixed_ref, scaled_ref):
        g = pl.program_id(1)
        u = u_ref[...]
        row = lax.broadcasted_iota(jnp.int32, u.shape, 0)
        s2 = u + _shift_down(u, 1, row)
        s4 = s2 + _shift_down(s2, 2, row)
        s8 = s4 + _shift_down(s4, 4, row)
        s16 = s8 + _shift_down(s8, 8, row)
        win = jnp.where(g == 0, s2, jnp.where(g == 1, s4, jnp.where(g == 2, s8, s16)))
        width = lax.shift_left(jnp.int32(2), g)
        cnt = jnp.minimum(row + 1, width).astype(F32)
        pooled = (win / cnt - u).astype(MX)
        pooled_ref[...] = pooled
        mixed = jnp.dot(pooled, w_ref[0], preferred_element_type=F32)
        mixed_ref[...] = mixed
        scaled_ref[...] = (mixed * s_ref[...]).astype(scaled_ref.dtype)

    return _pcall(body, name=name, grid=(nb, POOL_G),
                  in_specs=[blk, pl.BlockSpec((1, LANE, LANE), lambda b, g: (g, 0, 0)),
                            pl.BlockSpec((1, LANE), lambda b, g: (0, g))],
                  out_specs=[blk, blk, blk],
                  out_shape=[_sds((T, POOL_W), MX), _sds((T, POOL_W), F32), _sds((T, POOL_W), MX)])(z, pool_grp, pool_scale)


def _pool_bwd(dscaled, mixed, pooled, pool_grp, pool_scale, *, nb, name):
    T = dscaled.shape[0]
    S = T // nb
    blk = pl.BlockSpec((S, LANE), lambda g, b: (b, g))

    def body(ds_ref, mixed_ref, pooled_ref, w_ref, s_ref, du_ref, dw_ref, dsc_ref):
        g = pl.program_id(0)
        b = pl.program_id(1)
        ds = ds_ref[...]
        dsc_ref[0] = jnp.sum(ds * mixed_ref[...], axis=0, keepdims=True)
        dmixed = (ds * s_ref[...]).astype(MX)
        dw = lax.dot_general(pooled_ref[...], dmixed, (((0,), (0,)), ((), ())), preferred_element_type=F32)

        @pl.when(b == 0)
        def _():
            dw_ref[0] = dw

        @pl.when(b > 0)
        def _():
            dw_ref[0] += dw
        dpooled = lax.dot_general(dmixed, w_ref[0], (((1,), (1,)), ((), ())), preferred_element_type=F32)
        row = lax.broadcasted_iota(jnp.int32, dpooled.shape, 0)
        width = lax.shift_left(jnp.int32(2), g)
        q = dpooled / jnp.minimum(row + 1, width).astype(F32)
        r2 = q + _shift_up(q, 1, row, S)
        r4 = r2 + _shift_up(r2, 2, row, S)
        r8 = r4 + _shift_up(r4, 4, row, S)
        r16 = r8 + _shift_up(r8, 8, row, S)
        win = jnp.where(g == 0, r2, jnp.where(g == 1, r4, jnp.where(g == 2, r8, r16)))
        du_ref[...] = (win - dpooled).astype(du_ref.dtype)

    return _pcall(body, name=name, grid=(POOL_G, nb),
                  in_specs=[blk, blk, blk, pl.BlockSpec((1, LANE, LANE), lambda g, b: (g, 0, 0)),
                            pl.BlockSpec((1, LANE), lambda g, b: (0, g))],
                  out_specs=[blk, pl.BlockSpec((1, LANE, LANE), lambda g, b: (g, 0, 0)),
                             pl.BlockSpec((1, 1, LANE), lambda g, b: (b, 0, g))],
                  out_shape=[_sds((T, POOL_W), MX), _sds((POOL_G, LANE, LANE), F32), _sds((nb, 1, POOL_W), F32)],
                  )(dscaled, mixed, pooled, pool_grp, pool_scale)


def _lat_fwd(z, q_gain, kv_gain, *, name, tm=256):
    T = z.shape[0]

    def body(z_ref, qg_ref, kg_ref, qn_ref, kvn_ref):
        ql = z_ref[:, Z_QL:Z_QL + QL]
        kv = z_ref[:, Z_KV:Z_KV + KVL]
        qn_ref[...] = (ql * _rsq(ql) * qg_ref[...]).astype(qn_ref.dtype)
        kvn_ref[...] = (kv * _rsq(kv) * kg_ref[...]).astype(kvn_ref.dtype)

    return _pcall(body, name=name, grid=(T // tm,),
                  in_specs=[_row_spec(tm, LAT_W), pl.BlockSpec((1, QL), lambda i: (0, 0)), pl.BlockSpec((1, KVL), lambda i: (0, 0))],
                  out_specs=[_row_spec(tm, QL), _row_spec(tm, KVL)],
                  out_shape=[_sds((T, QL), MX), _sds((T, KVL), MX)])(z, q_gain, kv_gain)


def _lat_bwd(z, dqn, dkvn, q_gain, kv_gain, *, nb, name, tm=256):
    T = z.shape[0]
    tps = (T // nb) // tm

    def norm_bwd(x, dy, gain):
        r = _rsq(x)
        xhat = x * r
        dgain = jnp.sum(dy * xhat, axis=0, keepdims=True)
        dxhat = dy * gain
        return r * (dxhat - xhat * jnp.mean(dxhat * xhat, axis=-1, keepdims=True)), dgain

    def body(z_ref, dq_ref, dkv_ref, qg_ref, kg_ref, dql_ref, dkvl_ref, sq_ref, sk_ref):
        i = pl.program_id(0)
        dql, dqg = norm_bwd(z_ref[:, Z_QL:Z_QL + QL], dq_ref[...], qg_ref[...])
        dkvl, dkg = norm_bwd(z_ref[:, Z_KV:Z_KV + KVL], dkv_ref[...], kg_ref[...])
        dql_ref[...] = dql.astype(dql_ref.dtype)
        dkvl_ref[...] = dkvl.astype(dkvl_ref.dtype)

        @pl.when(i % tps == 0)
        def _():
            sq_ref[...] = jnp.zeros_like(sq_ref)
            sk_ref[...] = jnp.zeros_like(sk_ref)

        sq_ref[0, 0:1, :] += dqg
        sk_ref[0, 0:1, :] += dkg

    return _pcall(body, name=name, grid=(T // tm,),
                  in_specs=[_row_spec(tm, LAT_W), _row_spec(tm, QL), _row_spec(tm, KVL),
                            pl.BlockSpec((1, QL), lambda i: (0, 0)), pl.BlockSpec((1, KVL), lambda i: (0, 0))],
                  out_specs=[_row_spec(tm, QL), _row_spec(tm, KVL),
                             pl.BlockSpec((1, SUBLANE, QL), lambda i: (i // tps, 0, 0)),
                             pl.BlockSpec((1, SUBLANE, KVL), lambda i: (i // tps, 0, 0))],
                  out_shape=[_sds((T, QL), MX), _sds((T, KVL), MX), _sds((nb, SUBLANE, QL), F32), _sds((nb, SUBLANE, KVL), F32)],
                  )(z, dqn, dkvn, q_gain, kv_gain)


def _lane_masks(shape):
    lane = lax.broadcasted_iota(jnp.int32, shape, len(shape) - 1)
    m_n = lane < NOPE
    m_r = jnp.logical_and(lane >= KR_LANE, lane < KR_LANE + ROPE)
    first_half = lane < KR_LANE + ROPE // 2
    return m_n, m_r, first_half


def _rot(y, first_half):
    return jnp.where(first_half, -pltpu.roll(y, LANE - ROPE // 2, 1), pltpu.roll(y, ROPE // 2, 1))


def _rot_t(v, first_half, m_r):
    return jnp.where(m_r, jnp.where(first_half, pltpu.roll(v, LANE - ROPE // 2, 1), -pltpu.roll(v, ROPE // 2, 1)), 0.0)


def _prep_fwd(qp, kvp, z, cos, sin, q_gain, k_gain, *, name, tm=256):
    T = qp.shape[0]
    slab = pl.BlockSpec((tm, LANE), lambda i: (i, 0))
    kr_spec = pl.BlockSpec((tm, LANE), lambda i: (i, Z_KR // LANE))
    vec = pl.BlockSpec((1, LANE), lambda i: (0, 0))

    def body(qp_ref, kvp_ref, kr_ref, cos_ref, sin_ref, qg_ref, kg_ref, q_ref, k_ref, v_ref):
        m_n, m_r, first_half = _lane_masks((tm, LANE))
        c = cos_ref[...]
        s = sin_ref[...]
        qg = qg_ref[...]
        kg = kg_ref[...]
        xr = kr_ref[...]
        rr = lax.rsqrt(jnp.sum(xr * xr, axis=-1, keepdims=True) * (1.0 / ROPE) + EPS)
        yr = xr * rr * kg
        kr = jnp.where(m_r, yr * c + _rot(yr, first_half) * s, 0.0)
        for h in range(NH):
            x = qp_ref[:, h * LANE:(h + 1) * LANE]
            x2 = x * x
            rn = lax.rsqrt(jnp.sum(jnp.where(m_n, x2, 0.0), axis=-1, keepdims=True) * (1.0 / NOPE) + EPS)
            rq = lax.rsqrt(jnp.sum(jnp.where(m_r, x2, 0.0), axis=-1, keepdims=True) * (1.0 / ROPE) + EPS)
            y = x * jnp.where(m_n, rn, jnp.where(m_r, rq, 0.0)) * qg
            q_ref[:, h * LANE:(h + 1) * LANE] = (y * c + _rot(y, first_half) * s).astype(q_ref.dtype)
            xk = kvp_ref[:, h * LANE:(h + 1) * LANE]
            rk = lax.rsqrt(jnp.sum(jnp.where(m_n, xk * xk, 0.0), axis=-1, keepdims=True) * (1.0 / NOPE) + EPS)
            k_ref[:, h * LANE:(h + 1) * LANE] = (jnp.where(m_n, xk * rk * kg, 0.0) + kr).astype(k_ref.dtype)
        v_ref[...] = kvp_ref[:, NH * LANE:].astype(v_ref.dtype)

    return _pcall(body, name=name, grid=(T // tm,),
                  in_specs=[_row_spec(tm, NH * LANE), _row_spec(tm, NH * LANE + NH * NOPE), kr_spec, slab, slab, vec, vec],
                  out_specs=[_row_spec(tm, NH * LANE), _row_spec(tm, NH * LANE), _row_spec(tm, NH * NOPE)],
                  out_shape=[_sds((T, NH * LANE), MX), _sds((T, NH * LANE), MX), _sds((T, NH * NOPE), MX)],
                  )(qp, kvp, z, cos, sin, q_gain, k_gain)


def _prep_bwd(dq, dk, dv, qp, kvp, z, cos, sin, q_gain, k_gain, *, nb, name, tm=256):
    T = qp.shape[0]
    tps = (T // nb) // tm
    slab = pl.BlockSpec((tm, LANE), lambda i: (i, 0))
    kr_spec = pl.BlockSpec((tm, LANE), lambda i: (i, Z_KR // LANE))
    vec = pl.BlockSpec((1, LANE), lambda i: (0, 0))

    def body(dq_ref, dk_ref, dv_ref, qp_ref, kvp_ref, kr_ref, cos_ref, sin_ref, qg_ref, kg_ref,
             dqp_ref, dkvp_ref, dkr_ref, st_ref):
        i = pl.program_id(0)
        m_n, m_r, first_half = _lane_masks((tm, LANE))
        c = cos_ref[...]
        s = sin_ref[...]
        qg = qg_ref[...]
        kg = kg_ref[...]
        dqg = jnp.zeros((1, LANE), F32)
        dkg = jnp.zeros((1, LANE), F32)
        dkr_sum = jnp.zeros((tm, LANE), F32)
        for h in range(NH):
            x = qp_ref[:, h * LANE:(h + 1) * LANE]
            x2 = x * x
            rn = lax.rsqrt(jnp.sum(jnp.where(m_n, x2, 0.0), axis=-1, keepdims=True) * (1.0 / NOPE) + EPS)
            rq = lax.rsqrt(jnp.sum(jnp.where(m_r, x2, 0.0), axis=-1, keepdims=True) * (1.0 / ROPE) + EPS)
            rfac = jnp.where(m_n, rn, jnp.where(m_r, rq, 0.0))
            xhat = x * rfac
            do = dq_ref[:, h * LANE:(h + 1) * LANE]
            dy = do * c + _rot_t(do * s, first_half, m_r)
            dqg = dqg + jnp.sum(dy * xhat, axis=0, keepdims=True)
            dxhat = dy * qg
            t = dxhat * xhat
            mean_n = jnp.sum(jnp.where(m_n, t, 0.0), axis=-1, keepdims=True) * (1.0 / NOPE)
            mean_r = jnp.sum(jnp.where(m_r, t, 0.0), axis=-1, keepdims=True) * (1.0 / ROPE)
            dqp_ref[:, h * LANE:(h + 1) * LANE] = (
                rfac * (dxhat - xhat * jnp.where(m_n, mean_n, jnp.where(m_r, mean_r, 0.0)))).astype(dqp_ref.dtype)

            xk = kvp_ref[:, h * LANE:(h + 1) * LANE]
            rk = lax.rsqrt(jnp.sum(jnp.where(m_n, xk * xk, 0.0), axis=-1, keepdims=True) * (1.0 / NOPE) + EPS)
            khat = jnp.where(m_n, xk * rk, 0.0)
            dko = dk_ref[:, h * LANE:(h + 1) * LANE]
            dkn = jnp.where(m_n, dko, 0.0)
            dkg = dkg + jnp.sum(dkn * khat, axis=0, keepdims=True)
            dkhat = dkn * kg
            mean_k = jnp.sum(dkhat * khat, axis=-1, keepdims=True) * (1.0 / NOPE)
            dkvp_ref[:, h * LANE:(h + 1) * LANE] = jnp.where(m_n, rk * (dkhat - khat * mean_k), 0.0).astype(dkvp_ref.dtype)
            dkr_sum = dkr_sum + jnp.where(m_r, dko, 0.0)
        dkvp_ref[:, NH * LANE:] = dv_ref[...].astype(dkvp_ref.dtype)

        xr = kr_ref[...]
        rr = lax.rsqrt(jnp.sum(xr * xr, axis=-1, keepdims=True) * (1.0 / ROPE) + EPS)
        rhat = xr * rr
        dyr = dkr_sum * c + _rot_t(dkr_sum * s, first_half, m_r)
        dkg = dkg + jnp.sum(dyr * rhat, axis=0, keepdims=True)
        drhat = dyr * kg
        mean_kr = jnp.sum(drhat * rhat, axis=-1, keepdims=True) * (1.0 / ROPE)
        dkr_ref[...] = jnp.where(m_r, rr * (drhat - rhat * mean_kr), 0.0).astype(dkr_ref.dtype)

        @pl.when(i % tps == 0)
        def _():
            st_ref[...] = jnp.zeros_like(st_ref)

        st_ref[0, 0:1, :] += dqg
        st_ref[0, 1:2, :] += dkg

    return _pcall(body, name=name, grid=(T // tm,),
                  in_specs=[_row_spec(tm, NH * LANE), _row_spec(tm, NH * LANE), _row_spec(tm, NH * NOPE),
                            _row_spec(tm, NH * LANE), _row_spec(tm, NH * LANE + NH * NOPE), kr_spec, slab, slab, vec, vec],
                  out_specs=[_row_spec(tm, NH * LANE), _row_spec(tm, NH * LANE + NH * NOPE), slab,
                             pl.BlockSpec((1, SUBLANE, LANE), lambda i: (i // tps, 0, 0))],
                  out_shape=[_sds((T, NH * LANE), MX), _sds((T, NH * LANE + NH * NOPE), MX), _sds((T, LANE), MX),
                             _sds((nb, SUBLANE, LANE), F32)],
                  )(dq, dk, dv, qp, kvp, z, cos, sin, q_gain, k_gain)


def _lower_triangle(t):
    return lax.broadcasted_iota(jnp.int32, (t, t), 1) <= lax.broadcasted_iota(jnp.int32, (t, t), 0)


def _attn_fwd(q, k, v, *, nb, name, tq=512, comm=None):
    T = q.shape[0]
    S = T // nb
    tq = _pick(S, tq)
    nq = S // tq
    tk = tq
    npair = NH // 2

    def body(q_ref, k_ref, v_ref, o_ref, lse_ref):
        qi = pl.program_id(2)
        lane = lax.broadcasted_iota(jnp.int32, (tq, LANE), 1)
        qs = [q_ref[:, hh * LANE:(hh + 1) * LANE] for hh in range(2)]

        def block(j, carry, diagonal):
            k0 = pl.multiple_of(j * tk, tk)
            vb = v_ref[pl.ds(k0, tk), :]
            new = []
            for hh in range(2):
                m, l, acc = carry[hh]
                kb = k_ref[pl.ds(k0, tk), hh * LANE:(hh + 1) * LANE]
                s = lax.dot_general(qs[hh], kb, (((1,), (1,)), ((), ())), preferred_element_type=F32) * ATTN_SCALE
                if diagonal:
                    s = jnp.where(_lower_triangle(tq), s, NEG)
                m_new = jnp.maximum(m, jnp.max(s, axis=-1, keepdims=True))
                p = jnp.exp(s - m_new)
                alpha = jnp.exp(m - m_new)
                l = alpha * l + jnp.sum(p, axis=-1, keepdims=True)
                acc = alpha * acc + jnp.dot(p.astype(MX), vb, preferred_element_type=F32)
                new.append((m_new, l, acc))
            return tuple(new)

        init = tuple((jnp.full((tq, 1), NEG, F32), jnp.zeros((tq, 1), F32), jnp.zeros((tq, LANE), F32)) for _ in range(2))
        carry = lax.fori_loop(0, qi, lambda j, c: block(j, c, False), init)
        (m0, l0, acc0), (m1, l1, acc1) = block(qi, carry, True)
        o_ref[...] = jnp.where(lane < NOPE, acc0 / l0, acc1 / l1).astype(o_ref.dtype)
        lse_ref[...] = jnp.where(lane < NOPE, m0 + jnp.log(l0), m1 + jnp.log(l1))

    return _pcall(body, name=name, grid=(nb, npair, nq),
                  in_specs=[pl.BlockSpec((tq, 2 * LANE), lambda b, p, i: (b * nq + i, p)),
                            pl.BlockSpec((S, 2 * LANE), lambda b, p, i: (b, p)),
                            pl.BlockSpec((S, LANE), lambda b, p, i: (b, p))],
                  out_specs=[pl.BlockSpec((tq, LANE), lambda b, p, i: (b * nq + i, p)),
                             pl.BlockSpec((tq, LANE), lambda b, p, i: (b * nq + i, p))],
                  out_shape=[_sds((T, NH * NOPE), MX), _sds((T, NH * NOPE), F32)], comm=comm)(q, k, v)


def _attn_bwd(q, k, v, o, lse, do, *, nb, name, tq=512, comm=None):
    T = q.shape[0]
    S = T // nb
    tq = _pick(S, tq)
    nq = S // tq
    tk = tq
    npair = NH // 2

    def body(q_ref, k_ref, v_ref, o_ref, lse_ref, do_ref, dq_ref, dk_ref, dv_ref, delta_ref):
        lane = lax.broadcasted_iota(jnp.int32, (tq, LANE), 1)
        first = lane < NOPE
        dq_ref[...] = jnp.zeros_like(dq_ref)

        def delta_step(qi, _):
            q0 = pl.multiple_of(qi * tq, tq)
            prod = do_ref[pl.ds(q0, tq), :] * o_ref[pl.ds(q0, tq), :].astype(F32)
            d0 = jnp.sum(jnp.where(first, prod, 0.0), axis=-1, keepdims=True)
            d1 = jnp.sum(jnp.where(first, 0.0, prod), axis=-1, keepdims=True)
            delta_ref[pl.ds(q0, tq), :] = jnp.where(first, d0, d1)
            return 0

        lax.fori_loop(0, nq, delta_step, 0)

        def kv_step(kj, _):
            k0 = pl.multiple_of(kj * tk, tk)
            kbs = [k_ref[pl.ds(k0, tk), hh * LANE:(hh + 1) * LANE] for hh in range(2)]
            vb = v_ref[pl.ds(k0, tk), :]

            def q_block(qi, carry, diagonal):
                dk0, dk1, dv = carry
                dks = [dk0, dk1]
                q0 = pl.multiple_of(qi * tq, tq)
                dov = do_ref[pl.ds(q0, tq), :]
                lse_v = lse_ref[pl.ds(q0, tq), :]
                delta_v = delta_ref[pl.ds(q0, tq), :]
                for hh in range(2):
                    qh = q_ref[pl.ds(q0, tq), hh * LANE:(hh + 1) * LANE]
                    dob = jnp.where(first if hh == 0 else jnp.logical_not(first), dov, 0.0).astype(MX)
                    s = lax.dot_general(qh, kbs[hh], (((1,), (1,)), ((), ())), preferred_element_type=F32) * ATTN_SCALE
                    p = jnp.exp(s - lse_v[:, hh * NOPE:hh * NOPE + 1])
                    if diagonal:
                        p = jnp.where(_lower_triangle(tq), p, 0.0)
                    dp = lax.dot_general(dob, vb, (((1,), (1,)), ((), ())), preferred_element_type=F32)
                    ds = (p * (dp - delta_v[:, hh * NOPE:hh * NOPE + 1]) * ATTN_SCALE).astype(MX)
                    dks[hh] = dks[hh] + lax.dot_general(ds, qh, (((0,), (0,)), ((), ())), preferred_element_type=F32)
                    dv = dv + lax.dot_general(p.astype(MX), dob, (((0,), (0,)), ((), ())), preferred_element_type=F32)
                    dq_ref[pl.ds(q0, tq), hh * LANE:(hh + 1) * LANE] += jnp.dot(ds, kbs[hh], preferred_element_type=F32)
                return dks[0], dks[1], dv

            zero = jnp.zeros((tk, LANE), F32)
            carry = q_block(kj, (zero, zero, zero), True)
            dk0, dk1, dv = lax.fori_loop(kj + 1, nq, lambda qi, c: q_block(qi, c, False), carry)
            dk_ref[pl.ds(k0, tk), 0:LANE] = dk0
            dk_ref[pl.ds(k0, tk), LANE:2 * LANE] = dk1
            dv_ref[pl.ds(k0, tk), :] = dv
            return 0

        lax.fori_loop(0, nq, kv_step, 0)

    pair256 = pl.BlockSpec((S, 2 * LANE), lambda b, p: (b, p))
    pair128 = pl.BlockSpec((S, LANE), lambda b, p: (b, p))
    return _pcall(body, name=name, grid=(nb, npair),
                  in_specs=[pair256, pair256, pair128, pair128, pair128, pair128],
                  out_specs=[pair256, pair256, pair128],
                  out_shape=[_sds((T, NH * LANE), F32), _sds((T, NH * LANE), F32), _sds((T, NH * NOPE), F32)],
                  scratch=[pltpu.VMEM((S, LANE), F32)], comm=comm)(q, k, v, o, lse, do)


class _NoExchange:
    def __init__(self, weights):
        self.W, self.G = weights, {}

    def rider(self, name):
        return None

    def landed(self, name, outs):
        pass

    def grad(self, key, g):
        self.G[key] = g


def _run(plan, fn, *args, name, **kw):
    rider = plan.rider(name)
    if rider is None:
        return fn(*args, name=name, **kw)
    outs, landed = fn(*args, name=name, comm=rider, **kw)
    plan.landed(name, landed)
    return outs


def _layer_fwd_bwd(x, tgt, mod3, cos, sin, plan, P):
    nb = mod3.shape[0]
    W = plan.W
    h1, gu1, a1 = _run(plan, _ffn_up_normed, x, mod3, P["norm_ffn1"], W["ffn1_in"], sub=0, name="ffn1_up")
    f1 = _mm(a1, W["ffn1_out"], mode="nn", out_dtype=F32, name="ffn1_out", tm=1024, tn=1024, tk=DFF)
    x1, h2 = _fwd_block(x, f1, mod3, P["norm_mix"], sub=1, coef=0.5, name="fwd_norm2")
    z = _mm(h2, W["w_in"], mode="nn", out_dtype=F32, name="mix_in", tn=1664)
    pooled, mixed, scaled = _pool_fwd(z, W["pool_grp"], P["pool_scale"], nb=nb, name="pool_fwd")
    br_pool = _mm(scaled, W["pool_proj"], mode="nn", out_dtype=F32, name="pool_proj")
    qn, kvn = _lat_fwd(z, P["q_a_norm"], P["kv_a_norm"], name="lat_fwd")
    qp = _mm(qn, W["q_up"], mode="nn", out_dtype=F32, name="q_up")
    kvp = _mm(kvn, W["kv_up"], mode="nn", out_dtype=F32, name="kv_up")
    q, k, v = _prep_fwd(qp, kvp, z, cos, sin, P["q_gain"], P["k_gain"], name="prep_fwd")
    attn, lse = _run(plan, _attn_fwd, q, k, v, nb=nb, name="attn_fwd")
    br_mla = _mm(attn, W["mla_proj"], mode="nn", out_dtype=F32, name="mla_proj")
    merged = _merge_fwd(z, br_pool, br_mla, name="merge_fwd")
    mo = _mm(merged, W["w_out"], mode="nn", out_dtype=F32, name="mix_out")
    x2, h3 = _fwd_block(x1, mo, mod3, P["norm_ffn2"], sub=2, coef=1.0, name="fwd_norm3")
    gu2, a2 = _ffn_up(h3, W["ffn2_in"], name="ffn2_up")
    f2 = _mm(a2, W["ffn2_out"], mode="nn", out_dtype=F32, name="ffn2_out", tm=1024, tn=1024, tk=DFF)
    dy, df2, st_fin, loss = _final(x2, f2, tgt, mod3, name="loss_head")

    plan.grad("ffn2_out", _dw(a2, df2, name="d_ffn2_out"))
    dgu2 = _ffn_dact(df2, gu2, W["ffn2_out"], name="ffn2_dact")
    dh3 = _ffn_dh(dgu2, W["ffn2_in"], name="d_ffn2_h")
    plan.grad("ffn2_in", _ffn_dw_in(h3, dgu2, name="d_ffn2_in"))
    dx2, dmo, st3 = _run(plan, _bwd_block, x2, dh3, dy, mo, mod3, P["norm_ffn2"], sub=2, coef=1.0, name="bwd_norm3")
    plan.grad("w_out", _dw(merged, dmo, name="d_mix_out"))
    dmerged = _mm(dmo, W["w_out"], mode="nt", out_dtype=F32, name="d_merged")
    dbr_pool, dbr_mla, dgates = _merge_bwd(z, br_pool, br_mla, dmerged, name="merge_bwd")
    plan.grad("pool_proj", _dw(scaled, dbr_pool, name="d_pool_proj"))
    dscaled = _mm(dbr_pool, W["pool_proj"], mode="nt", out_dtype=F32, name="d_pool_scaled")
    du_pool, d_pool_grp, d_pool_scale = _pool_bwd(dscaled, mixed, pooled, W["pool_grp"], P["pool_scale"], nb=nb, name="pool_bwd")
    plan.grad("mla_proj", _dw(attn, dbr_mla, name="d_mla_proj"))
    dattn = _mm(dbr_mla, W["mla_proj"], mode="nt", out_dtype=F32, name="d_attn")
    dq, dk, dv = _run(plan, _attn_bwd, q, k, v, attn, lse, dattn, nb=nb, name="attn_bwd")
    dqp, dkvp, dkr, st_prep = _prep_bwd(dq, dk, dv, qp, kvp, z, cos, sin, P["q_gain"], P["k_gain"], nb=nb, name="prep_bwd")
    plan.grad("q_up", _dw(qn, dqp, name="d_q_up"))
    dqn = _mm(dqp, W["q_up"], mode="nt", out_dtype=F32, name="d_qn")
    plan.grad("kv_up", _dw(kvn, dkvp, name="d_kv_up"))
    dkvn = _mm(dkvp, W["kv_up"], mode="nt", out_dtype=F32, name="d_kvn")
    dql, dkvl, st_q, st_kv = _lat_bwd(z, dqn, dkvn, P["q_a_norm"], P["kv_a_norm"], nb=nb, name="lat_bwd")
    dz = jnp.concatenate([du_pool, dql, dkvl, dkr, dgates], axis=1)
    plan.grad("w_in", _dw(h2, dz, name="d_mix_in", tn=1664))
    dh2 = _run(plan, _mm, dz, W["w_in"], mode="nt", out_dtype=F32, name="d_mix_h")
    dx1, df1, st2 = _bwd_block(x1, dh2, dx2, f1, mod3, P["norm_mix"], sub=1, coef=0.5, name="bwd_norm2")
    plan.grad("ffn1_out", _dw(a1, df1, name="d_ffn1_out"))
    dgu1 = _run(plan, _ffn_dact, df1, gu1, W["ffn1_out"], name="ffn1_dact")
    dh1 = _run(plan, _ffn_dh, dgu1, W["ffn1_in"], name="d_ffn1_h")
    plan.grad("ffn1_in", _run(plan, _ffn_dw_in, h1, dgu1, name="d_ffn1_in"))
    grad_x, st1 = _run(plan, _bwd_block, x, dh1, dx1, None, mod3, P["norm_ffn1"], sub=0, coef=0.0, name="bwd_norm1")

    return loss, grad_x, (st1, st2, st3, st_fin, st_q, st_kv, st_prep, d_pool_scale), d_pool_grp


def _w_in_to_kernel(w):
    k = w.shape[0]
    zeros = lambda n: jnp.zeros((k, n), w.dtype)
    return jnp.concatenate([w[:, 0:1152], zeros(KR_LANE), w[:, 1152:1184], zeros(LANE - KR_LANE - ROPE), w[:, 1184:]], axis=1)


def _w_in_from_kernel(g):
    return jnp.concatenate([g[:, 0:1152], g[:, Z_KR + KR_LANE:Z_KR + KR_LANE + ROPE], g[:, Z_GP:]], axis=1)


def _q_up_to_kernel(w):
    k = w.shape[0]
    return jnp.pad(w.reshape(k, NH, NOPE + ROPE), ((0, 0), (0, 0), (0, LANE - NOPE - ROPE))).reshape(k, NH * LANE)


def _q_up_from_kernel(g):
    k = g.shape[0]
    return g.reshape(k, NH, LANE)[:, :, :NOPE + ROPE].reshape(k, NH * (NOPE + ROPE))


def _kv_up_to_kernel(w):
    k = w.shape[0]
    w3 = w.reshape(k, NH, 2 * NOPE)
    kpart = jnp.pad(w3[:, :, :NOPE], ((0, 0), (0, 0), (0, LANE - NOPE))).reshape(k, NH * LANE)
    return jnp.concatenate([kpart, w3[:, :, NOPE:].reshape(k, NH * NOPE)], axis=1)


def _kv_up_from_kernel(g):
    k = g.shape[0]
    kpart = g[:, :NH * LANE].reshape(k, NH, LANE)[:, :, :NOPE]
    vpart = g[:, NH * LANE:].reshape(k, NH, NOPE)
    return jnp.concatenate([kpart, vpart], axis=2).reshape(k, NH * 2 * NOPE)


def _gain_slab(nope, rope):
    return jnp.concatenate([nope, rope, jnp.zeros((1, LANE - NOPE - ROPE), nope.dtype)], axis=1)


def _rope_tables(positions):
    inv_freq = 10000.0 ** (-jnp.arange(0, ROPE, 2, dtype=F32) / ROPE)
    ang = positions.astype(F32)[:, None] * inv_freq
    ang = jnp.concatenate([ang, ang], axis=-1)
    t = positions.shape[0]
    cos = jnp.concatenate([jnp.ones((t, KR_LANE), F32), jnp.cos(ang), jnp.ones((t, LANE - KR_LANE - ROPE), F32)], axis=1)
    sin = jnp.concatenate([jnp.zeros((t, KR_LANE), F32), jnp.sin(ang), jnp.zeros((t, LANE - KR_LANE - ROPE), F32)], axis=1)
    return cos, sin


def _coords():
    return lax.axis_index("x"), lax.axis_index("y"), lax.axis_index("c")


HBM_SPEC = pl.BlockSpec(memory_space=pl.ANY)
VMEM_SPEC = pl.BlockSpec(memory_space=pltpu.VMEM)


def _gather8(v, *, name, comm=None):
    r, c = v.shape

    def body(v_ref, out_ref, send_sems, recv_sems, local_sem):
        x, y, cc = _coords()
        me = 4 * x + 2 * y + cc
        mine = pltpu.make_async_copy(v_ref, out_ref.at[me], local_sem)
        mine.start()
        copies = []
        for kk in range(1, N_DEV):
            peer = (x ^ (kk >> 2), y ^ ((kk >> 1) & 1), cc ^ (kk & 1))
            cp = pltpu.make_async_remote_copy(src_ref=v_ref, dst_ref=out_ref.at[me], send_sem=send_sems.at[kk - 1],
                                              recv_sem=recv_sems.at[kk - 1], device_id=peer, device_id_type=MESH)
            cp.start()
            copies.append(cp)
        for kk in range(1, N_DEV):
            peer_slot = me ^ kk
            pltpu.make_async_remote_copy(src_ref=v_ref, dst_ref=out_ref.at[peer_slot], send_sem=send_sems.at[kk - 1],
                                         recv_sem=recv_sems.at[kk - 1], device_id=(x, y, cc), device_id_type=MESH).wait_recv()
        for cp in copies:
            cp.wait_send()
        mine.wait()

    return _pcall(body, name=name, out_shape=_sds((N_DEV, r, c), v.dtype), in_specs=[VMEM_SPEC], out_specs=VMEM_SPEC,
                  scratch=[pltpu.SemaphoreType.DMA((N_DEV - 1,)), pltpu.SemaphoreType.DMA((N_DEV - 1,)), pltpu.SemaphoreType.DMA],
                  comm=comm)(v)


CHIP_RELS = ((1, 0), (0, 1), (1, 1))


def _comm_only(comm, *, name):
    _, outs = _pcall(lambda: None, name=name, out_shape=[], in_specs=[], out_specs=[], comm=comm)()
    return outs


def _gather_comm(shards):
    n = len(shards)

    def ici(ins, outs, sems, a, j, x, y, cc):
        half = ins[a].shape[0] // 2
        mine = pl.ds(cc * half, half)
        dx, dy = CHIP_RELS[j]
        return pltpu.make_async_remote_copy(src_ref=ins[a].at[mine], dst_ref=outs[a].at[2 * x + y, mine],
                                            send_sem=sems[0].at[a, j], recv_sem=sems[1].at[a, j],
                                            device_id=(x ^ dx, y ^ dy, cc), device_id_type=MESH)

    def d2d(ins, outs, sems, a, j, x, y, cc, half_of):
        half = ins[a].shape[0] // 2
        dx, dy = CHIP_RELS[j]
        landed = outs[a].at[2 * (x ^ dx) + (y ^ dy), pl.ds(half_of * half, half)]
        return pltpu.make_async_remote_copy(src_ref=landed, dst_ref=landed, send_sem=sems[2].at[a, j], recv_sem=sems[3].at[a, j],
                                            device_id=(x, y, 1 - cc), device_id_type=MESH)

    def own(ins, outs, sems, a, x, y):
        return pltpu.make_async_copy(ins[a], outs[a].at[2 * x + y], sems[4].at[a])

    def start(ins, outs, sems):
        x, y, cc = _coords()
        for a in range(n):
            own(ins, outs, sems, a, x, y).start()
            for j in range(3):
                ici(ins, outs, sems, a, j, x, y, cc).start()

    def finish(ins, outs, sems):
        x, y, cc = _coords()
        for a in range(n):
            for j in range(3):
                ici(ins, outs, sems, a, j, x, y, cc).wait_recv()
                d2d(ins, outs, sems, a, j, x, y, cc, cc).start()
        for a in range(n):
            for j in range(3):
                d2d(ins, outs, sems, a, j, x, y, cc, 1 - cc).wait_recv()
        for a in range(n):
            for j in range(3):
                ici(ins, outs, sems, a, j, x, y, cc).wait_send()
                d2d(ins, outs, sems, a, j, x, y, cc, cc).wait_send()
            own(ins, outs, sems, a, x, y).wait()

    dma = pltpu.SemaphoreType.DMA
    return _Comm(shards, [_sds((N_CHIP,) + s.shape, s.dtype) for s in shards],
                 [dma((n, 3)), dma((n, 3)), dma((n, 3)), dma((n, 3)), dma((n,))], start, finish)


def _swap_comm(parts):
    n = len(parts)

    def copy(ins, outs, sems, a):
        x, y, cc = _coords()
        half = ins[a].shape[1] // 2
        return pltpu.make_async_remote_copy(src_ref=ins[a].at[:, pl.ds((1 - cc) * half, half)], dst_ref=outs[a],
                                            send_sem=sems[0].at[a], recv_sem=sems[1].at[a], device_id=(x, y, 1 - cc),
                                            device_id_type=MESH)

    def start(ins, outs, sems):
        for a in range(n):
            copy(ins, outs, sems, a).start()

    def finish(ins, outs, sems):
        for a in range(n):
            copy(ins, outs, sems, a).wait()

    dma = pltpu.SemaphoreType.DMA
    return _Comm(parts, [_sds((p.shape[0], p.shape[1] // 2, p.shape[2]), p.dtype) for p in parts], [dma((n,)), dma((n,))],
                 start, finish)


def _add_half(full, other, cidx, *, name):
    nch, r, c = full.shape
    half = r // 2
    tr = _pick_rows(half)
    nbk = half // tr
    grid_spec = pltpu.PrefetchScalarGridSpec(
        num_scalar_prefetch=1, grid=(nch, nbk),
        in_specs=[pl.BlockSpec((1, tr, c), lambda j, i, cref: (j, cref[0] * nbk + i, 0)),
                  pl.BlockSpec((1, tr, c), lambda j, i, cref: (j, i, 0))],
        out_specs=pl.BlockSpec((1, tr, c), lambda j, i, cref: (j, i, 0)))

    def body(cref, a_ref, b_ref, o_ref):
        o_ref[...] = (a_ref[...] + b_ref[...]).astype(o_ref.dtype)

    return _pcall(body, name=name, out_shape=_sds((nch, half, c), MX), grid_spec=grid_spec)(cidx, full, other)


def _pick_rows(rows, target=512):
    best = None
    for t in range(16, min(rows, target) + 1, 16):
        if rows % t == 0:
            best = t
    return rows if best is None else best


def _exchange_comm(parts):
    n = len(parts)

    def send(ins, outs, sems, a, j, x, y, cc):
        dx, dy = CHIP_RELS[j]
        return pltpu.make_async_remote_copy(src_ref=ins[a].at[2 * (x ^ dx) + (y ^ dy)], dst_ref=outs[a].at[2 * x + y],
                                            send_sem=sems[0].at[a, j], recv_sem=sems[1].at[a, j],
                                            device_id=(x ^ dx, y ^ dy, cc), device_id_type=MESH)

    def landing(ins, outs, sems, a, j, x, y, cc):
        dx, dy = CHIP_RELS[j]
        peer_chip = 2 * (x ^ dx) + (y ^ dy)
        return pltpu.make_async_remote_copy(src_ref=ins[a].at[peer_chip], dst_ref=outs[a].at[peer_chip], send_sem=sems[0].at[a, j],
                                            recv_sem=sems[1].at[a, j], device_id=(x, y, cc), device_id_type=MESH)

    def own(ins, outs, sems, a, x, y):
        return pltpu.make_async_copy(ins[a].at[2 * x + y], outs[a].at[2 * x + y], sems[2].at[a])

    def start(ins, outs, sems):
        x, y, cc = _coords()
        for a in range(n):
            own(ins, outs, sems, a, x, y).start()
            for j in range(3):
                send(ins, outs, sems, a, j, x, y, cc).start()

    def finish(ins, outs, sems):
        x, y, cc = _coords()
        for a in range(n):
            for j in range(3):
                landing(ins, outs, sems, a, j, x, y, cc).wait_recv()
        for a in range(n):
            for j in range(3):
                send(ins, outs, sems, a, j, x, y, cc).wait_send()
            own(ins, outs, sems, a, x, y).wait()

    dma = pltpu.SemaphoreType.DMA
    return _Comm(parts, [_sds(p.shape, p.dtype) for p in parts], [dma((n, 3)), dma((n, 3)), dma((n,))], start, finish)


def _sum_chips(q, cidx, *, name):
    nch, h, c = q.shape
    tr = _pick_rows(h)
    nbk = h // tr
    grid_spec = pltpu.PrefetchScalarGridSpec(
        num_scalar_prefetch=1, grid=(nbk,),
        in_specs=[pl.BlockSpec((nch, tr, c), lambda i, cref: (0, i, 0))],
        out_specs=pl.BlockSpec((tr, c), lambda i, cref: (cref[0] * nbk + i, 0)))

    def body(cref, q_ref, o_ref):
        acc = q_ref[0].astype(F32) + q_ref[1].astype(F32)
        acc = acc + q_ref[2].astype(F32)
        o_ref[...] = acc + q_ref[3].astype(F32)

    return _pcall(body, name=name, out_shape=_sds((2 * h, c), F32), grid_spec=grid_spec)(cidx, q)


def _join_comm(fulls):
    n = len(fulls)

    def half_copy(outs, sems, a, which):
        x, y, cc = _coords()
        h = outs[a].shape[0] // 2
        rows = outs[a].at[pl.ds((cc if which == 0 else 1 - cc) * h, h)]
        return pltpu.make_async_remote_copy(src_ref=rows, dst_ref=rows, send_sem=sems[0].at[a], recv_sem=sems[1].at[a],
                                            device_id=(x, y, 1 - cc), device_id_type=MESH)

    def start(ins, outs, sems):
        for a in range(n):
            half_copy(outs, sems, a, 0).start()

    def finish(ins, outs, sems):
        for a in range(n):
            half_copy(outs, sems, a, 1).wait_recv()
        for a in range(n):
            half_copy(outs, sems, a, 0).wait_send()

    dma = pltpu.SemaphoreType.DMA
    return _Comm(fulls, [_sds(p.shape, p.dtype) for p in fulls], [dma((n,)), dma((n,))], start, finish,
                 aliases={a: a for a in range(n)})


def _ada_prologue(c, w, b, *, name, comm=None):
    nb, dm = c.shape
    n = w.shape[1]

    def body(c_ref, w_ref, b_ref, call_ref, land_ref, cpad, mod_s, g_send, g_recv, m_send, m_recv):
        x, y, cc = _coords()
        me = 4 * x + 2 * y + cc
        chip = 2 * x + y
        cpad[...] = jnp.zeros_like(cpad)
        cpad[0:nb, :] = c_ref[...]
        copies = []
        for kk in range(1, N_DEV):
            peer = (x ^ (kk >> 2), y ^ ((kk >> 1) & 1), cc ^ (kk & 1))
            cp = pltpu.make_async_remote_copy(src_ref=cpad, dst_ref=call_ref.at[me], send_sem=g_send.at[kk - 1],
                                              recv_sem=g_recv.at[kk - 1], device_id=peer, device_id_type=MESH)
            cp.start()
            copies.append(cp)
        call_ref[me] = cpad[...]
        for kk in range(1, N_DEV):
            pltpu.make_async_remote_copy(src_ref=cpad, dst_ref=call_ref.at[me ^ kk], send_sem=g_send.at[kk - 1],
                                         recv_sem=g_recv.at[kk - 1], device_id=(x, y, cc), device_id_type=MESH).wait_recv()
        cv = call_ref[...].reshape(N_DEV * SUBLANE, dm)
        act = (cv * _sigmoid(cv)).astype(MX)
        mod = jnp.dot(act, w_ref[...].astype(MX), preferred_element_type=F32) + b_ref[...]
        mod_s[...] = mod.reshape(N_DEV, SUBLANE, n)
        for j, (dx, dy) in enumerate(CHIP_RELS):
            cp = pltpu.make_async_remote_copy(src_ref=mod_s.at[4 * (x ^ dx) + 2 * (y ^ dy) + cc], dst_ref=land_ref.at[chip],
                                              send_sem=m_send.at[j], recv_sem=m_recv.at[j],
                                              device_id=(x ^ dx, y ^ dy, cc), device_id_type=MESH)
            cp.start()
            copies.append(cp)
        land_ref[chip] = mod_s[me]
        for j, (dx, dy) in enumerate(CHIP_RELS):
            pltpu.make_async_remote_copy(src_ref=mod_s.at[me], dst_ref=land_ref.at[2 * (x ^ dx) + (y ^ dy)], send_sem=m_send.at[j],
                                         recv_sem=m_recv.at[j], device_id=(x, y, cc), device_id_type=MESH).wait_recv()
        for cp in copies:
            cp.wait_send()

    dma = pltpu.SemaphoreType.DMA
    return _pcall(body, name=name, in_specs=[VMEM_SPEC] * 3, out_specs=[VMEM_SPEC] * 2,
                  out_shape=[_sds((N_DEV, SUBLANE, dm), F32), _sds((N_CHIP, SUBLANE, n), F32)],
                  scratch=[pltpu.VMEM((SUBLANE, dm), F32), pltpu.VMEM((N_DEV, SUBLANE, n), F32),
                           dma((N_DEV - 1,)), dma((N_DEV - 1,)), dma((3,)), dma((3,))], comm=comm)(c, w, b)


def _ada_bwd(c_all, dmod_cols, *, name):
    m, kdim = c_all.shape
    n = dmod_cols.shape[1]
    tn = _pick(n, 1152)

    def body(c_ref, d_ref, o_ref):
        cv = c_ref[...]
        act = (cv * _sigmoid(cv)).astype(MX)
        o_ref[...] = lax.dot_general(act, d_ref[...].astype(MX), (((0,), (0,)), ((), ())), preferred_element_type=F32)

    return _pcall(body, name=name, out_shape=_sds((kdim, n), F32), grid=(n // tn,),
                  in_specs=[pl.BlockSpec((m, kdim), lambda j: (0, 0)), pl.BlockSpec((m, tn), lambda j: (0, j))],
                  out_specs=pl.BlockSpec((kdim, tn), lambda j: (0, j)))(c_all, dmod_cols)


SLAB_W = 1024
RED_ROWS = 8


LOSS_LANE = SLAB_W - LANE


def _pack_stats(st1, st2, st3, st_fin, st_q, st_kv, st_prep, d_pool_scale, loss8, *, name):
    nb = st1.shape[0]
    dm_rows = -(-9 * nb // SUBLANE) * SUBLANE

    def body(s1, s2, s3, sf, sq, skv, sp, sps, loss_ref, o_ref):
        o_ref[...] = jnp.zeros_like(o_ref)
        for s in range(nb):
            rows = [s1[s, 0:1, :], s1[s, 1:2, :], s2[s, 3:4, :], s2[s, 0:1, :], s2[s, 1:2, :], s3[s, 3:4, :], s3[s, 0:1, :],
                    s3[s, 1:2, :], sf[s, 0:1, :]]
            for k, row in enumerate(rows):
                o_ref[9 * s + k:9 * s + k + 1, :] = row

        def over_seq(ref, r):
            acc = ref[0, r:r + 1, :]
            for s in range(1, nb):
                acc = acc + ref[s, r:r + 1, :]
            return acc

        o_ref[dm_rows + 0:dm_rows + 1, :] = over_seq(s1, 2)
        o_ref[dm_rows + 1:dm_rows + 2, :] = over_seq(s2, 2)
        o_ref[dm_rows + 2:dm_rows + 3, :] = over_seq(s3, 2)
        o_ref[dm_rows + 3:dm_rows + 4, 0:POOL_W] = over_seq(sps, 0)
        o_ref[dm_rows + 4:dm_rows + 5, 0:QL] = over_seq(sq, 0)
        o_ref[dm_rows + 5:dm_rows + 6, 0:KVL] = over_seq(skv, 0)
        o_ref[dm_rows + 6:dm_rows + 7, 0:LANE] = over_seq(sp, 0)
        o_ref[dm_rows + 7:dm_rows + 8, 0:LANE] = over_seq(sp, 1)
        o_ref[dm_rows + 7:dm_rows + 8, LOSS_LANE:] = loss_ref[0:1, :]

    return _pcall(body, name=name, out_shape=_sds((dm_rows + RED_ROWS, SLAB_W), F32), in_specs=[VMEM_SPEC] * 9,
                  out_specs=VMEM_SPEC)(st1, st2, st3, st_fin, st_q, st_kv, st_prep, d_pool_scale, loss8)


def _small_allreduce(slab, pool, *, name, comm=None):
    rows, w = slab.shape
    dm = rows - RED_ROWS
    prow, pw = pool.shape
    crow = RED_ROWS + 2 * dm

    def body(slab_ref, pool_ref, red_ref, dmod_ref, ptot_ref, sib_slab, sib_pool, chip_slab, chip_pool, land_slab, land_pool,
             a_send, a_recv, b_send, b_recv):
        x, y, cc = _coords()
        chip = 2 * x + y
        sib = (x, y, 1 - cc)
        to_sib = [pltpu.make_async_remote_copy(src_ref=slab_ref, dst_ref=sib_slab, send_sem=a_send.at[0], recv_sem=a_recv.at[0],
                                               device_id=sib, device_id_type=MESH),
                  pltpu.make_async_remote_copy(src_ref=pool_ref, dst_ref=sib_pool, send_sem=a_send.at[1], recv_sem=a_recv.at[1],
                                               device_id=sib, device_id_type=MESH)]
        for cp in to_sib:
            cp.start()
        for cp in to_sib:
            cp.wait()
        mine_dm, theirs_dm = slab_ref[0:dm, :], sib_slab[0:dm, :]
        chip_slab[0:RED_ROWS, :] = slab_ref[dm:, :] + sib_slab[dm:, :]
        chip_slab[RED_ROWS:RED_ROWS + dm, :] = jnp.where(cc == 0, mine_dm, theirs_dm)
        chip_slab[RED_ROWS + dm:, :] = jnp.where(cc == 0, theirs_dm, mine_dm)
        chip_pool[...] = pool_ref[...] + sib_pool[...]

        sends = []
        for j, (dx, dy) in enumerate(CHIP_RELS):
            peer = (x ^ dx, y ^ dy, cc)
            sends.append(pltpu.make_async_remote_copy(src_ref=chip_slab, dst_ref=land_slab.at[chip], send_sem=b_send.at[j, 0],
                                                      recv_sem=b_recv.at[j, 0], device_id=peer, device_id_type=MESH))
            sends.append(pltpu.make_async_remote_copy(src_ref=chip_pool, dst_ref=land_pool.at[chip], send_sem=b_send.at[j, 1],
                                                      recv_sem=b_recv.at[j, 1], device_id=peer, device_id_type=MESH))
        for cp in sends:
            cp.start()
        land_slab[chip] = chip_slab[...]
        land_pool[chip] = chip_pool[...]
        for j, (dx, dy) in enumerate(CHIP_RELS):
            peer_chip = 2 * (x ^ dx) + (y ^ dy)
            pltpu.make_async_remote_copy(src_ref=chip_slab, dst_ref=land_slab.at[peer_chip], send_sem=b_send.at[j, 0],
                                         recv_sem=b_recv.at[j, 0], device_id=(x, y, cc), device_id_type=MESH).wait_recv()
            pltpu.make_async_remote_copy(src_ref=chip_pool, dst_ref=land_pool.at[peer_chip], send_sem=b_send.at[j, 1],
                                         recv_sem=b_recv.at[j, 1], device_id=(x, y, cc), device_id_type=MESH).wait_recv()
        red = land_slab[0, 0:RED_ROWS, :]
        ptot = land_pool[0]
        for ch in range(1, N_CHIP):
            red = red + land_slab[ch, 0:RED_ROWS, :]
            ptot = ptot + land_pool[ch]
        red_ref[...] = red
        ptot_ref[...] = ptot
        for ch in range(N_CHIP):
            dmod_ref[2 * dm * ch:2 * dm * (ch + 1), :] = land_slab[ch, RED_ROWS:, :]
        for cp in sends:
            cp.wait_send()

    dma = pltpu.SemaphoreType.DMA
    return _pcall(body, name=name, in_specs=[VMEM_SPEC, VMEM_SPEC], out_specs=[VMEM_SPEC] * 3,
                  out_shape=[_sds((RED_ROWS, w), F32), _sds((N_DEV * dm, w), F32), _sds((prow, pw), F32)],
                  scratch=[pltpu.VMEM((rows, w), F32), pltpu.VMEM((prow, pw), F32), pltpu.VMEM((crow, w), F32),
                           pltpu.VMEM((prow, pw), F32), pltpu.VMEM((N_CHIP, crow, w), F32), pltpu.VMEM((N_CHIP, prow, pw), F32),
                           dma((2,)), dma((2,)), dma((3, 2)), dma((3, 2))], comm=comm)(slab, pool)


def _adamw_math(w, g, m, v):
    mn = ADAM_B1 * m + (1.0 - ADAM_B1) * g
    vn = ADAM_B2 * v + (1.0 - ADAM_B2) * (g * g)
    bc1 = 1.0 / (1.0 - ADAM_B1 ** ADAM_STEP)
    bc2 = 1.0 / (1.0 - ADAM_B2 ** ADAM_STEP)
    return -ADAM_LR * ((mn * bc1) / (jnp.sqrt(vn * bc2) + ADAM_EPS) + ADAM_WD * w), mn, vn


def _small_update(red, dmod_all, pool_total, nb, params, *, name):
    names = list(SMALL)
    dm = dmod_all.shape[0] // N_DEV

    def grad_of(nm, red_ref, dmod_ref, ptot_ref):
        if nm == "b_ada":
            acc = None
            for d in range(N_DEV):
                for s in range(nb):
                    blk = dmod_ref[d * dm + 9 * s:d * dm + 9 * s + 9, :]
                    acc = blk if acc is None else acc + blk
            return jnp.concatenate([acc[k:k + 1, :] for k in range(9)], axis=1)
        if nm == "pool_grp":
            return ptot_ref[...]
        row, lo, n = {"norm_ffn1": (0, 0, D), "norm_mix": (1, 0, D), "norm_ffn2": (2, 0, D), "pool_scale": (3, 0, POOL_W),
                      "q_a_norm": (4, 0, QL), "kv_a_norm": (5, 0, KVL), "q_norm_nope": (6, 0, NOPE),
                      "q_norm_rope": (6, NOPE, ROPE), "k_norm_nope": (7, 0, NOPE), "k_norm_rope": (7, KR_LANE, ROPE)}[nm]
        return red_ref[row:row + 1, lo:lo + n]

    def body(*refs):
        red_ref, dmod_ref, ptot_ref = refs[:3]
        ins = refs[3:3 + 3 * len(names)]
        outs = refs[3 + 3 * len(names):]
        outs[4 * len(names)][...] = red_ref[RED_ROWS - 1:RED_ROWS, LOSS_LANE:]
        for i, nm in enumerate(names):
            g = grad_of(nm, red_ref, dmod_ref, ptot_ref)
            d, mn, vn = _adamw_math(ins[3 * i][...], g, ins[3 * i + 1][...], ins[3 * i + 2][...])
            outs[4 * i][...] = g
            outs[4 * i + 1][...] = d
            outs[4 * i + 2][...] = mn
            outs[4 * i + 3][...] = vn

    flat_in = [a for nm in names for a in params[nm]]
    out_shape = [_sds(params[nm][0].shape, F32) for nm in names for _ in range(4)] + [_sds((1, LANE), F32)]
    res = _pcall(body, name=name, in_specs=[VMEM_SPEC] * (3 + len(flat_in)), out_specs=[VMEM_SPEC] * len(out_shape),
                 out_shape=out_shape)(red, dmod_all, pool_total, *flat_in)
    return {nm: tuple(res[4 * i:4 * i + 4]) for i, nm in enumerate(names)}, res[-1]


def _adamw(w, g, m, v, *, name, comm=None):
    r, c = w.shape
    tr = _pick_rows(r, 256)
    tc = c if tr < r else _pick(c, 256)
    spec = pl.BlockSpec((tr, tc), lambda i, j: (i, j))
    bc1 = 1.0 / (1.0 - ADAM_B1 ** ADAM_STEP)
    bc2 = 1.0 / (1.0 - ADAM_B2 ** ADAM_STEP)

    def body(w_ref, g_ref, m_ref, v_ref, d_ref, mo_ref, vo_ref):
        gv = g_ref[...]
        mn = ADAM_B1 * m_ref[...] + (1.0 - ADAM_B1) * gv
        vn = ADAM_B2 * v_ref[...] + (1.0 - ADAM_B2) * (gv * gv)
        mo_ref[...] = mn
        vo_ref[...] = vn
        d_ref[...] = -ADAM_LR * ((mn * bc1) / (jnp.sqrt(vn * bc2) + ADAM_EPS) + ADAM_WD * w_ref[...])

    out = _sds((r, c), F32)
    return _pcall(body, name=name, out_shape=[out, out, out], grid=(r // tr, c // tc), in_specs=[spec] * 4, out_specs=[spec] * 3,
                  comm=comm)(w, g, m, v)


BIG = ("w_ffn1_in", "w_ffn1_out", "w_in", "w_pool_proj", "w_q_up", "w_kv_up", "w_mla_proj", "w_out", "w_ffn2_in", "w_ffn2_out")
ROW_SHARDED = ("w_ffn1_out", "w_out", "w_ffn2_out")
KERNEL_NAME = {"w_ffn1_in": "ffn1_in", "w_ffn1_out": "ffn1_out", "w_in": "w_in", "w_pool_proj": "pool_proj", "w_q_up": "q_up",
               "w_kv_up": "kv_up", "w_mla_proj": "mla_proj", "w_out": "w_out", "w_ffn2_in": "ffn2_in", "w_ffn2_out": "ffn2_out"}
WEIGHTS = ("w_ada", "b_ada", "norm_ffn1", "w_ffn1_in", "w_ffn1_out", "norm_mix", "w_in", "pool_grp", "pool_scale", "w_pool_proj",
           "q_a_norm", "w_q_up", "kv_a_norm", "w_kv_up", "q_norm_nope", "q_norm_rope", "k_norm_nope", "k_norm_rope", "w_mla_proj",
           "w_out", "norm_ffn2", "w_ffn2_in", "w_ffn2_out")
SMALL = ("b_ada", "norm_ffn1", "norm_mix", "pool_grp", "pool_scale", "q_a_norm", "kv_a_norm", "q_norm_nope", "q_norm_rope",
         "k_norm_nope", "k_norm_rope", "norm_ffn2")
SMALL_SEQ = tuple(n for n in SMALL if n != "pool_grp")
SLAB_W = 1024


def _assemble(name, stacked):
    if name in ROW_SHARDED:
        return stacked.reshape(stacked.shape[0] * stacked.shape[1], stacked.shape[2])
    return jnp.transpose(stacked, (1, 0, 2)).reshape(stacked.shape[1], stacked.shape[0] * stacked.shape[2])


def _split(name, full):
    if name in ROW_SHARDED:
        return full.reshape(N_CHIP, full.shape[0] // N_CHIP, full.shape[1])
    return jnp.transpose(full.reshape(full.shape[0], N_CHIP, full.shape[1] // N_CHIP), (1, 0, 2))


GU = ("w_ffn1_in", "w_ffn2_in")
NARROW = ("w_in", "w_q_up")


def _to_rows(v, width=SLAB_W):
    flat = v.reshape(-1)
    rows = -(-flat.shape[0] // width)
    return jnp.pad(flat, (0, rows * width - flat.shape[0])).reshape(rows, width)


def _pack_small(parts, names=SMALL):
    rows, spans, at = [], {}, 0
    for name in names:
        r = _to_rows(parts[name])
        spans[name] = (at, parts[name].size, parts[name].shape)
        rows.append(r)
        at += r.shape[0]
    pad = (-at) % SUBLANE
    if pad:
        rows.append(jnp.zeros((pad, SLAB_W), F32))
    return jnp.concatenate(rows, axis=0), spans


def _unpack_small(slab, spans):
    out = {}
    for name, (at, size, shape) in spans.items():
        nrow = -(-size // SLAB_W)
        out[name] = slab[at:at + nrow].reshape(-1)[:size].reshape(shape)
    return out


GATHER_ON = {"ada_prologue": ("w_ffn1_in", "w_ffn1_out"),
             "ffn1_up": ("w_in", "w_pool_proj", "w_q_up", "w_kv_up", "w_mla_proj", "w_out"),
             "attn_fwd": ("w_ffn2_in", "w_ffn2_out")}
SWAP_ON = {"bwd_norm3": ("w_ffn2_out", "w_ffn2_in"),
           "d_mix_h": ("w_out", "w_pool_proj", "w_mla_proj", "w_q_up", "w_kv_up", "w_in"),
           "ffn1_dact": ("w_ffn1_out",),
           "bwd_norm1": ("w_ffn1_in",)}
EXCHANGE_ON = {"attn_bwd": "bwd_norm3", "d_ffn1_h": "d_mix_h", "d_ffn1_in": "ffn1_dact", "small_allreduce": "bwd_norm1"}


class _ExchangePlan:
    def __init__(self, shards, cidx):
        self.shards, self.cidx = shards, cidx
        self.W, self.G, self.parts, self.pre, self.reduced = {}, {}, {}, {}, {}

    def grad(self, key, g):
        self.G[key] = g

    def rider(self, name):
        if name in GATHER_ON:
            return _gather_comm([self.shards[n] for n in GATHER_ON[name]])
        if name in SWAP_ON:
            self.parts[name] = [self._stacked(n) for n in SWAP_ON[name]]
            return _swap_comm(self.parts[name])
        if name in EXCHANGE_ON:
            return _exchange_comm(self.pre[EXCHANGE_ON[name]])
        return None

    def landed(self, name, outs):
        if name in GATHER_ON:
            for n, g in zip(GATHER_ON[name], outs):
                self.W[KERNEL_NAME[n]] = self._to_kernel(n, g)
        elif name in SWAP_ON:
            self.pre[name] = [_add_half(p, o, self.cidx, name="rs_add_" + KERNEL_NAME[n])
                              for n, p, o in zip(SWAP_ON[name], self.parts[name], outs)]
        elif name in EXCHANGE_ON:
            for n, q in zip(SWAP_ON[EXCHANGE_ON[name]], outs):
                self.reduced[n] = _sum_chips(q, self.cidx, name="rs_sum_" + KERNEL_NAME[n])

    @staticmethod
    def _to_kernel(n, stacked):
        if n in GU:
            return stacked
        full = _assemble(n, stacked)
        return {"w_in": _w_in_to_kernel, "w_q_up": _q_up_to_kernel, "w_kv_up": _kv_up_to_kernel}.get(n, lambda w: w)(full)

    def _stacked(self, n):
        g = self.G[KERNEL_NAME[n]]
        if n in GU:
            return g
        full = {"w_in": _w_in_from_kernel, "w_q_up": _q_up_from_kernel, "w_kv_up": _kv_up_from_kernel}.get(n, lambda w: w)(g)
        return _split(n, full)


def kernel(x, c, positions, w_ada, b_ada, norm_ffn1, w_ffn1_in, w_ffn1_out, norm_mix, w_in, pool_grp, pool_scale, w_pool_proj, q_a_norm, w_q_up, kv_a_norm, w_kv_up, q_norm_nope, q_norm_rope, k_norm_nope, k_norm_rope, w_mla_proj, w_out, norm_ffn2, w_ffn2_in, w_ffn2_out, loss_target, m_w_ada, m_b_ada, m_norm_ffn1, m_w_ffn1_in, m_w_ffn1_out, m_norm_mix, m_w_in, m_pool_grp, m_pool_scale, m_w_pool_proj, m_q_a_norm, m_w_q_up, m_kv_a_norm, m_w_kv_up, m_q_norm_nope, m_q_norm_rope, m_k_norm_nope, m_k_norm_rope, m_w_mla_proj, m_w_out, m_norm_ffn2, m_w_ffn2_in, m_w_ffn2_out, v_w_ada, v_b_ada, v_norm_ffn1, v_w_ffn1_in, v_w_ffn1_out, v_norm_mix, v_w_in, v_pool_grp, v_pool_scale, v_w_pool_proj, v_q_a_norm, v_w_q_up, v_kv_a_norm, v_w_kv_up, v_q_norm_nope, v_q_norm_rope, v_k_norm_nope, v_k_norm_rope, v_w_mla_proj, v_w_out, v_norm_ffn2, v_w_ffn2_in, v_w_ffn2_out):
    args = dict(locals())
    wts = {n: args[n][0] for n in WEIGHTS}
    mom = {n: args["m_" + n][0] for n in WEIGHTS}
    var = {n: args["v_" + n][0] for n in WEIGHTS}
    nb, seq, dm = x.shape
    tokens = nb * seq
    xi, yi, ci = _coords()
    chip = 2 * xi + yi
    dev = 4 * xi + 2 * yi + ci

    cidx = ci.astype(jnp.int32).reshape(1)
    plan = _ExchangePlan({n: wts[n].astype(MX) for n in BIG}, cidx)
    plan.W["pool_grp"] = wts["pool_grp"].astype(MX)

    ncol = w_ada.shape[2]
    b_cols = lax.dynamic_slice_in_dim(wts["b_ada"].reshape(1, -1), chip * ncol, ncol, axis=1)
    c_slots, mod_slots = _run(plan, _ada_prologue, c, wts["w_ada"], b_cols, name="ada_prologue")
    c_all = c_slots[:, :nb].reshape(N_DEV * nb, dm)
    mod3 = jnp.transpose(mod_slots[:, :nb], (1, 0, 2)).reshape(nb, 9, dm)
    P = {"norm_ffn1": wts["norm_ffn1"].reshape(1, dm), "norm_mix": wts["norm_mix"].reshape(1, dm),
         "norm_ffn2": wts["norm_ffn2"].reshape(1, dm), "pool_scale": wts["pool_scale"].reshape(1, POOL_W),
         "q_a_norm": wts["q_a_norm"].reshape(1, QL), "kv_a_norm": wts["kv_a_norm"].reshape(1, KVL),
         "q_gain": _gain_slab(wts["q_norm_nope"].reshape(1, NOPE), wts["q_norm_rope"].reshape(1, ROPE)),
         "k_gain": _gain_slab(wts["k_norm_nope"].reshape(1, NOPE), wts["k_norm_rope"].reshape(1, ROPE))}
    cos, sin = _rope_tables(positions.reshape(tokens))

    loss8, grad_x, stats, d_pool_grp = _layer_fwd_bwd(x.reshape(tokens, dm), loss_target.reshape(tokens, dm), mod3, cos, sin, plan, P)

    slab = _pack_stats(*stats, loss8, name="pack_stats")
    red, dmod_rows, pool_total = _run(plan, _small_allreduce, slab, d_pool_grp.reshape(POOL_G * LANE, LANE), name="small_allreduce")
    joined = _comm_only(_join_comm([plan.reduced[n] for n in BIG]), name="rs_join_halves")
    grads = {n: g for n, g in zip(BIG, joined)}
    dm_rows = dmod_rows.shape[0] // N_DEV
    dmod_all = dmod_rows.reshape(N_DEV, dm_rows, SLAB_W)[:, :9 * nb].reshape(N_DEV * nb, 9 * dm)
    dmod_cols = lax.dynamic_slice_in_dim(dmod_all, chip * ncol, ncol, axis=1)
    grads["w_ada"] = _ada_bwd(c_all, dmod_cols, name="ada_bwd")

    delta, new_m, new_v = {}, {}, {}
    as2d = lambda a: a.reshape(POOL_G * LANE, LANE) if a.ndim == 4 else a.reshape(1, -1)
    upd, loss_row = _small_update(red, dmod_rows, pool_total, nb,
                                  {n: tuple(as2d(args[p + n]) for p in ("", "m_", "v_")) for n in SMALL}, name="small_update")
    loss = loss_row[0, 0]
    for n in SMALL:
        grads[n], delta[n], new_m[n], new_v[n] = upd[n]
    for n in ("w_ada",) + BIG:
        if n in NARROW:
            res = _adamw(wts[n].T, grads[n].T, mom[n].T, var[n].T, name="adamw_" + n)
            delta[n], new_m[n], new_v[n] = (r.T for r in res)
        else:
            delta[n], new_m[n], new_v[n] = _adamw(wts[n], grads[n], mom[n], var[n], name="adamw_" + n)

    def lead(a, n):
        return a.reshape((1,) + wts[n].shape)

    return (loss, grad_x.reshape(nb, seq, dm), *[lead(grads[n], n) for n in WEIGHTS], *[lead(delta[n], n) for n in WEIGHTS],
            *[lead(new_m[n], n) for n in WEIGHTS], *[lead(new_v[n], n) for n in WEIGHTS])
```

```python
import functools
import math

import jax
import jax.numpy as jnp
from jax import lax
from jax.experimental import pallas as pl
from jax.experimental.pallas import tpu as pltpu

F32 = jnp.float32
MX = jnp.bfloat16

D = 1024
DFF = 2816
NH = 8
POOL_W = 512
POOL_G = 4
QL = 384
KVL = 256
ROPE = 32
NOPE = 64
LANE = 128
SUBLANE = 8
EPS = 1e-6
ATTN_SCALE = 1.0 / math.sqrt(96.0)
NEG = -1e30

Z_UP, Z_QL, Z_KV, Z_KR, Z_GP, Z_GM, Z_W = 0, 512, 896, 1152, 1280, 2304, 3328
KR_LANE = 64
LAT_W = 1280

ADAM_LR, ADAM_B1, ADAM_B2, ADAM_EPS, ADAM_WD, ADAM_STEP = 0.001, 0.9, 0.999, 1e-08, 0.01, 10

VMEM_LIMIT = 48 * 1024 * 1024
MESH = pl.DeviceIdType.MESH
N_DEV = 8
N_CHIP = 4


class _Comm:
    def __init__(self, ins, out_shapes, sems, start, finish, aliases=None):
        self.ins, self.out_shapes, self.sems = list(ins), list(out_shapes), list(sems)
        self.start, self.finish = start, finish
        self.aliases = aliases or {}


def _pcall(body, *, name, out_shape, grid=(), in_specs=None, out_specs=None, scratch=(), grid_spec=None, aliases=None,
           comm=None):
    params = pltpu.CompilerParams(vmem_limit_bytes=VMEM_LIMIT)
    kw = dict(name=name, compiler_params=params)
    if comm is None:
        if aliases:
            kw["input_output_aliases"] = aliases
        if grid_spec is not None:
            return pl.pallas_call(body, grid_spec=grid_spec, out_shape=out_shape, **kw)
        return pl.pallas_call(body, grid=grid, in_specs=in_specs, out_specs=out_specs, scratch_shapes=scratch,
                              out_shape=out_shape, **kw)
    single = not isinstance(out_shape, (list, tuple))
    outs = [out_shape] if single else list(out_shape)
    ospecs = [out_specs] if single else list(out_specs)
    n_in, n_out, n_ci, n_co, n_scr = len(in_specs), len(outs), len(comm.ins), len(comm.out_shapes), len(scratch)
    io = dict(aliases or {})
    io.update({n_in + i: n_out + o for i, o in comm.aliases.items()})

    def riding(*refs):
        ins, cins = refs[:n_in], refs[n_in:n_in + n_ci]
        at = n_in + n_ci
        os_, couts = refs[at:at + n_out], refs[at + n_out:at + n_out + n_co]
        at += n_out + n_co
        scr, csems = refs[at:at + n_scr], refs[at + n_scr:]
        if grid:
            first = functools.reduce(jnp.logical_and, [pl.program_id(d) == 0 for d in range(len(grid))])
            last = functools.reduce(jnp.logical_and, [pl.program_id(d) == grid[d] - 1 for d in range(len(grid))])
            pl.when(first)(lambda: comm.start(cins, couts, csems))
            body(*ins, *os_, *scr)
            pl.when(last)(lambda: comm.finish(cins, couts, csems))
        else:
            comm.start(cins, couts, csems)
            body(*ins, *os_, *scr)
            comm.finish(cins, couts, csems)

    call = pl.pallas_call(riding, grid=grid, in_specs=list(in_specs) + [HBM_SPEC] * n_ci, out_specs=ospecs + [HBM_SPEC] * n_co,
                          out_shape=outs + comm.out_shapes, scratch_shapes=list(scratch) + comm.sems,
                          input_output_aliases=io, **kw)

    def run(*args):
        res = call(*args, *comm.ins)
        main = list(res[:n_out])
        return (main[0] if single else main), list(res[n_out:])

    return run


def _pick(dim, target):
    best = None
    for t in range(LANE, min(dim, target) + 1, LANE):
        if dim % t == 0:
            best = t
    return dim if best is None else best


def _sds(shape, dtype):
    return jax.ShapeDtypeStruct(shape, dtype)


def _dw(a, g, *, name, tn=1024, comm=None):
    return _mm(a, g, mode="tn", out_dtype=F32, name=name, tm=256, tn=tn, tk=a.shape[0], n_outer=True, comm=comm)


def _mm(a, b, *, mode, out_dtype, name, tm=1024, tn=1024, tk=4096, n_outer=False, comm=None):
    if mode == "nn":
        (M, K), (K2, N) = a.shape, b.shape
    elif mode == "nt":
        (M, K), (N, K2) = a.shape, b.shape
    else:
        (K, M), (K2, N) = a.shape, b.shape
    assert K == K2, (name, a.shape, b.shape)
    tm, tn, tk = _pick(M, tm), _pick(N, tn), _pick(K, tk)
    nk = K // tk
    if n_outer:
        ij = lambda g0, g1: (g1, g0)
        grid = (N // tn, M // tm, nk)
    else:
        ij = lambda g0, g1: (g0, g1)
        grid = (M // tm, N // tn, nk)
    if mode == "tn":
        a_spec = pl.BlockSpec((tk, tm), lambda g0, g1, k: (k, ij(g0, g1)[0]))
    else:
        a_spec = pl.BlockSpec((tm, tk), lambda g0, g1, k: (ij(g0, g1)[0], k))
    if mode == "nt":
        b_spec = pl.BlockSpec((tn, tk), lambda g0, g1, k: (ij(g0, g1)[1], k))
    else:
        b_spec = pl.BlockSpec((tk, tn), lambda g0, g1, k: (k, ij(g0, g1)[1]))
    o_spec = pl.BlockSpec((tm, tn), lambda g0, g1, k: ij(g0, g1))
    dn = {"nn": (((1,), (0,)), ((), ())), "nt": (((1,), (1,)), ((), ())), "tn": (((0,), (0,)), ((), ()))}[mode]

    def dot(a_ref, b_ref):
        return lax.dot_general(a_ref[...].astype(MX), b_ref[...].astype(MX), dn, preferred_element_type=F32)

    def body_one(a_ref, b_ref, o_ref):
        o_ref[...] = dot(a_ref, b_ref).astype(o_ref.dtype)

    def body_acc(a_ref, b_ref, o_ref, acc_ref):
        k = pl.program_id(2)
        part = dot(a_ref, b_ref)

        @pl.when(k == 0)
        def _():
            acc_ref[...] = part

        @pl.when(k > 0)
        def _():
            acc_ref[...] += part

        @pl.when(k == nk - 1)
        def _():
            o_ref[...] = acc_ref[...].astype(o_ref.dtype)

    return _pcall(body_one if nk == 1 else body_acc, name=name, out_shape=_sds((M, N), out_dtype), grid=grid,
                  in_specs=[a_spec, b_spec], out_specs=o_spec, scratch=[] if nk == 1 else [pltpu.VMEM((tm, tn), F32)],
                  comm=comm)(a, b)


def _gu_shard(q):
    return (q % 2) * 2 + q // 2


def _ffn_up(h, w_st, *, name, tm=512, comm=None):
    T, dm = h.shape
    hw = w_st.shape[2]
    tm = _pick(T, tm)

    def body(h_ref, wg_ref, wu_ref, gu_ref, a_ref):
        hv = h_ref[...]
        g = jnp.dot(hv, wg_ref[0], preferred_element_type=F32)
        u = jnp.dot(hv, wu_ref[0], preferred_element_type=F32)
        gu_ref[:, :hw] = g.astype(gu_ref.dtype)
        gu_ref[:, hw:] = u.astype(gu_ref.dtype)
        a_ref[...] = (g * _sigmoid(g) * u).astype(a_ref.dtype)

    return _pcall(body, name=name, grid=(T // tm, 2),
                  in_specs=[pl.BlockSpec((tm, dm), lambda i, j: (i, 0)), pl.BlockSpec((1, dm, hw), lambda i, j: (j, 0, 0)),
                            pl.BlockSpec((1, dm, hw), lambda i, j: (2 + j, 0, 0))],
                  out_specs=[pl.BlockSpec((tm, 2 * hw), lambda i, j: (i, j)), pl.BlockSpec((tm, hw), lambda i, j: (i, j))],
                  out_shape=[_sds((T, 4 * hw), MX), _sds((T, 2 * hw), MX)], comm=comm)(h, w_st, w_st)


def _ffn_up_normed(x, mod3, gain, w_st, *, sub, name, tm=512, comm=None):
    T, dm = x.shape
    hw = w_st.shape[2]
    tm = _pick(T, tm)
    tps = (T // mod3.shape[0]) // tm

    def body(x_ref, mod_ref, n_ref, wg_ref, wu_ref, h_ref, gu_ref, a_ref):
        @pl.when(pl.program_id(1) == 0)
        def _():
            xv = x_ref[...]
            xn = xv * _rsq(xv) * n_ref[...]
            h_ref[...] = (xn * (1.0 + mod_ref[0, 3 * sub + 1:3 * sub + 2, :]) + mod_ref[0, 3 * sub:3 * sub + 1, :]).astype(h_ref.dtype)

        hv = h_ref[...]
        g = jnp.dot(hv, wg_ref[0], preferred_element_type=F32)
        u = jnp.dot(hv, wu_ref[0], preferred_element_type=F32)
        gu_ref[:, :hw] = g.astype(gu_ref.dtype)
        gu_ref[:, hw:] = u.astype(gu_ref.dtype)
        a_ref[...] = (g * _sigmoid(g) * u).astype(a_ref.dtype)

    return _pcall(body, name=name, grid=(T // tm, 2),
                  in_specs=[pl.BlockSpec((tm, dm), lambda i, j: (i, 0)), pl.BlockSpec((1, 9, dm), lambda i, j: (i // tps, 0, 0)),
                            pl.BlockSpec((1, dm), lambda i, j: (0, 0)), pl.BlockSpec((1, dm, hw), lambda i, j: (j, 0, 0)),
                            pl.BlockSpec((1, dm, hw), lambda i, j: (2 + j, 0, 0))],
                  out_specs=[pl.BlockSpec((tm, dm), lambda i, j: (i, 0)), pl.BlockSpec((tm, 2 * hw), lambda i, j: (i, j)),
                             pl.BlockSpec((tm, hw), lambda i, j: (i, j))],
                  out_shape=[_sds((T, dm), MX), _sds((T, 4 * hw), MX), _sds((T, 2 * hw), MX)],
                  comm=comm)(x, mod3, gain, w_st, w_st)


def _ffn_dact(df, gu, w_out, *, name, tm=512, comm=None):
    T, dm = df.shape
    hw = gu.shape[1] // 4
    tm = _pick(T, tm)

    def body(df_ref, gu_ref, wo_ref, dgu_ref):
        da = lax.dot_general(df_ref[...], wo_ref[...], (((1,), (1,)), ((), ())), preferred_element_type=F32)
        g = gu_ref[:, :hw].astype(F32)
        u = gu_ref[:, hw:].astype(F32)
        s = _sigmoid(g)
        dgu_ref[:, :hw] = (da * u * (s * (1.0 + g * (1.0 - s)))).astype(dgu_ref.dtype)
        dgu_ref[:, hw:] = (da * (g * s)).astype(dgu_ref.dtype)

    return _pcall(body, name=name, grid=(T // tm, 2),
                  in_specs=[pl.BlockSpec((tm, dm), lambda i, j: (i, 0)), pl.BlockSpec((tm, 2 * hw), lambda i, j: (i, j)),
                            pl.BlockSpec((hw, dm), lambda i, j: (j, 0))],
                  out_specs=pl.BlockSpec((tm, 2 * hw), lambda i, j: (i, j)), out_shape=_sds(gu.shape, MX),
                  comm=comm)(df, gu, w_out)


def _ffn_dh(dgu, w_st, *, name, tm=1024, comm=None):
    T = dgu.shape[0]
    _, dm, hw = w_st.shape
    tm = _pick(T, tm)

    def body(d_ref, w_ref, o_ref, acc_ref):
        q = pl.program_id(1)
        part = lax.dot_general(d_ref[...], w_ref[0], (((1,), (1,)), ((), ())), preferred_element_type=F32)

        @pl.when(q == 0)
        def _():
            acc_ref[...] = part

        @pl.when(q > 0)
        def _():
            acc_ref[...] += part

        @pl.when(q == 3)
        def _():
            o_ref[...] = acc_ref[...]

    return _pcall(body, name=name, grid=(T // tm, 4),
                  in_specs=[pl.BlockSpec((tm, hw), lambda i, q: (i, q)), pl.BlockSpec((1, dm, hw), lambda i, q: (_gu_shard(q), 0, 0))],
                  out_specs=pl.BlockSpec((tm, dm), lambda i, q: (i, 0)), out_shape=_sds((T, dm), F32),
                  scratch=[pltpu.VMEM((tm, dm), F32)], comm=comm)(dgu, w_st)


def _ffn_dw_in(h, dgu, *, name, tm=256, comm=None):
    T, dm = h.shape
    hw = dgu.shape[1] // 4
    tm = _pick(dm, tm)

    def body(h_ref, d_ref, o_ref):
        o_ref[0] = lax.dot_general(h_ref[...], d_ref[...], (((0,), (0,)), ((), ())), preferred_element_type=F32)

    return _pcall(body, name=name, grid=(4, dm // tm),
                  in_specs=[pl.BlockSpec((T, tm), lambda q, i: (0, i)), pl.BlockSpec((T, hw), lambda q, i: (0, q))],
                  out_specs=pl.BlockSpec((1, tm, hw), lambda q, i: (_gu_shard(q), i, 0)),
                  out_shape=_sds((4, dm, hw), F32), comm=comm)(h, dgu)


def _rsq(x):
    return lax.rsqrt(jnp.mean(x * x, axis=-1, keepdims=True) + EPS)


def _sigmoid(x):
    return 1.0 / (1.0 + jnp.exp(-x))


def _row_spec(tm, w):
    return pl.BlockSpec((tm, w), lambda i: (i, 0))


def _fwd_block(x_prev, f_prev, mod3, gain, *, sub, coef, name, tm=256):
    T, dm = x_prev.shape
    tps = (T // mod3.shape[0]) // tm
    has_f = f_prev is not None
    mod_spec = pl.BlockSpec((1, 9, dm), lambda i: (i // tps, 0, 0))
    vec_spec = pl.BlockSpec((1, dm), lambda i: (0, 0))

    def body(*refs):
        if has_f:
            x_ref, f_ref, mod_ref, n_ref, xo_ref, h_ref = refs
            x = x_ref[...] + coef * mod_ref[0, 3 * sub - 1:3 * sub, :] * f_ref[...]
            xo_ref[...] = x
        else:
            x_ref, mod_ref, n_ref, h_ref = refs
            x = x_ref[...]
        xn = x * _rsq(x) * n_ref[...]
        h = xn * (1.0 + mod_ref[0, 3 * sub + 1:3 * sub + 2, :]) + mod_ref[0, 3 * sub:3 * sub + 1, :]
        h_ref[...] = h.astype(h_ref.dtype)

    row = _row_spec(tm, dm)
    if has_f:
        return _pcall(body, name=name, grid=(T // tm,), in_specs=[row, row, mod_spec, vec_spec], out_specs=[row, row],
                      out_shape=[_sds((T, dm), F32), _sds((T, dm), MX)])(x_prev, f_prev, mod3, gain)
    return _pcall(body, name=name, grid=(T // tm,), in_specs=[row, mod_spec, vec_spec], out_specs=row,
                  out_shape=_sds((T, dm), MX))(x_prev, mod3, gain)


def _final(x2, f2, tgt, mod3, *, name, tm=256):
    T, dm = x2.shape
    tps = (T // mod3.shape[0]) // tm
    mod_spec = pl.BlockSpec((1, 9, dm), lambda i: (i // tps, 0, 0))
    row = _row_spec(tm, dm)
    stat_spec = pl.BlockSpec((1, SUBLANE, dm), lambda i: (i // tps, 0, 0))
    loss_spec = pl.BlockSpec((SUBLANE, LANE), lambda i: (0, 0))

    def body(x_ref, f_ref, t_ref, mod_ref, dy_ref, df_ref, st_ref, loss_ref):
        i = pl.program_id(0)
        g = mod_ref[0, 8:9, :]
        f = f_ref[...]
        err = x_ref[...] + 0.5 * g * f - t_ref[...]
        dy = err * (1.0 / dm)
        dy_ref[...] = dy
        df_ref[...] = (0.5 * g * dy).astype(df_ref.dtype)
        dgate = jnp.sum(0.5 * dy * f, axis=0, keepdims=True)
        part = 0.5 * jnp.sum(jnp.sum(err * err, axis=0, keepdims=True), axis=1, keepdims=True) * (1.0 / dm)

        @pl.when(i % tps == 0)
        def _():
            st_ref[...] = jnp.zeros_like(st_ref)

        @pl.when(i == 0)
        def _():
            loss_ref[...] = jnp.zeros_like(loss_ref)

        st_ref[0, 0:1, :] += dgate
        loss_ref[...] += jnp.broadcast_to(part, loss_ref.shape)

    return _pcall(body, name=name, grid=(T // tm,), in_specs=[row, row, row, mod_spec],
                  out_specs=[row, row, stat_spec, loss_spec],
                  out_shape=[_sds((T, dm), F32), _sds((T, dm), MX), _sds((mod3.shape[0], SUBLANE, dm), F32),
                             _sds((SUBLANE, LANE), F32)])(x2, f2, tgt, mod3)


def _bwd_block(x_cur, dh, dx_in, f_prev, mod3, gain, *, sub, coef, name, tm=256, comm=None):
    T, dm = x_cur.shape
    nb = mod3.shape[0]
    tps = (T // nb) // tm
    has_f = f_prev is not None
    mod_spec = pl.BlockSpec((1, 9, dm), lambda i: (i // tps, 0, 0))
    vec_spec = pl.BlockSpec((1, dm), lambda i: (0, 0))
    stat_spec = pl.BlockSpec((1, SUBLANE, dm), lambda i: (i // tps, 0, 0))
    row = _row_spec(tm, dm)

    def body(*refs):
        if has_f:
            x_ref, dh_ref, dxi_ref, f_ref, mod_ref, n_ref, dx_ref, df_ref, st_ref = refs
        else:
            x_ref, dh_ref, dxi_ref, mod_ref, n_ref, dx_ref, st_ref = refs
        i = pl.program_id(0)
        x = x_ref[...]
        r = _rsq(x)
        xhat = x * r
        n = n_ref[...]
        dhv = dh_ref[...]
        d_shift = jnp.sum(dhv, axis=0, keepdims=True)
        d_scale = jnp.sum(dhv * (xhat * n), axis=0, keepdims=True)
        dxn = dhv * (1.0 + mod_ref[0, 3 * sub + 1:3 * sub + 2, :])
        d_gain = jnp.sum(dxn * xhat, axis=0, keepdims=True)
        dxhat = dxn * n
        dx = dxi_ref[...] + r * (dxhat - xhat * jnp.mean(dxhat * xhat, axis=-1, keepdims=True))
        dx_ref[...] = dx

        @pl.when(i % tps == 0)
        def _():
            st_ref[...] = jnp.zeros_like(st_ref)

        st_ref[0, 0:1, :] += d_shift
        st_ref[0, 1:2, :] += d_scale
        st_ref[0, 2:3, :] += d_gain
        if has_f:
            f = f_ref[...]
            st_ref[0, 3:4, :] += jnp.sum(coef * dx * f, axis=0, keepdims=True)
            df_ref[...] = (coef * mod_ref[0, 3 * sub - 1:3 * sub, :] * dx).astype(df_ref.dtype)

    st_shape = _sds((nb, SUBLANE, dm), F32)
    if has_f:
        return _pcall(body, name=name, grid=(T // tm,), in_specs=[row, row, row, row, mod_spec, vec_spec],
                      out_specs=[row, row, stat_spec],
                      out_shape=[_sds((T, dm), F32), _sds((T, dm), MX), st_shape], comm=comm)(x_cur, dh, dx_in, f_prev, mod3, gain)
    return _pcall(body, name=name, grid=(T // tm,), in_specs=[row, row, row, mod_spec, vec_spec],
                  out_specs=[row, stat_spec], out_shape=[_sds((T, dm), F32), st_shape], comm=comm)(x_cur, dh, dx_in, mod3, gain)


def _merge_fwd(z, br_pool, br_mla, *, name, tm=256):
    T = z.shape[0]

    def body(z_ref, bp_ref, bm_ref, o_ref):
        gp = z_ref[:, Z_GP:Z_GP + D]
        gm = z_ref[:, Z_GM:Z_GM + D]
        o_ref[...] = (_sigmoid(gp) * bp_ref[...] + _sigmoid(gm) * bm_ref[...]).astype(o_ref.dtype)

    return _pcall(body, name=name, grid=(T // tm,), in_specs=[_row_spec(tm, Z_W), _row_spec(tm, D), _row_spec(tm, D)],
                  out_specs=_row_spec(tm, D), out_shape=_sds((T, D), MX))(z, br_pool, br_mla)


def _merge_bwd(z, br_pool, br_mla, dmerged, *, name, tm=256):
    T = z.shape[0]

    def body(z_ref, bp_ref, bm_ref, dm_ref, dbp_ref, dbm_ref, dg_ref):
        dm = dm_ref[...]
        sp = _sigmoid(z_ref[:, Z_GP:Z_GP + D])
        sm = _sigmoid(z_ref[:, Z_GM:Z_GM + D])
        dbp_ref[...] = (dm * sp).astype(dbp_ref.dtype)
        dbm_ref[...] = (dm * sm).astype(dbm_ref.dtype)
        dg_ref[:, :D] = (dm * bp_ref[...] * sp * (1.0 - sp)).astype(dg_ref.dtype)
        dg_ref[:, D:] = (dm * bm_ref[...] * sm * (1.0 - sm)).astype(dg_ref.dtype)

    return _pcall(body, name=name, grid=(T // tm,),
                  in_specs=[_row_spec(tm, Z_W), _row_spec(tm, D), _row_spec(tm, D), _row_spec(tm, D)],
                  out_specs=[_row_spec(tm, D), _row_spec(tm, D), _row_spec(tm, 2 * D)],
                  out_shape=[_sds((T, D), MX), _sds((T, D), MX), _sds((T, 2 * D), MX)])(z, br_pool, br_mla, dmerged)


def _shift_down(x, k, row):
    return jnp.where(row >= k, pltpu.roll(x, k, 0), 0.0)


def _shift_up(x, k, row, n):
    return jnp.where(row < n - k, pltpu.roll(x, n - k, 0), 0.0)


def _pool_fwd(z, pool_grp, pool_scale, *, nb, name):
    T = z.shape[0]
    S = T // nb
    blk = pl.BlockSpec((S, LANE), lambda b, g: (b, g))

    def body(u_ref, w_ref, s_ref, pooled_ref, mixed_ref, scaled_ref):
        g = pl.program_id(1)
        u = u_ref[...]
        row = lax.broadcasted_iota(jnp.int32, u.shape, 0)
        s2 = u + _shift_down(u, 1, row)
        s4 = s2 + _shift_down(s2, 2, row)
        s8 = s4 + _shift_down(s4, 4, row)
        s16 = s8 + _shift_down(s8, 8, row)
        win = jnp.where(g == 0, s2, jnp.where(g == 1, s4, jnp.where(g == 2, s8, s16)))
        width = lax.shift_left(jnp.int32(2), g)
        cnt = jnp.minimum(row + 1, width).astype(F32)
        pooled = (win / cnt - u).astype(MX)
        pooled_ref[...] = pooled
        mixed = jnp.dot(pooled, w_ref[0], preferred_element_type=F32)
        mixed_ref[...] = mixed
        scaled_ref[...] = (mixed * s_ref[...]).astype(scaled_ref.dtype)

    return _pcall(body, name=name, grid=(nb, POOL_G),
                  in_specs=[blk, pl.BlockSpec((1, LANE, LANE), lambda b, g: (g, 0, 0)),
                            pl.BlockSpec((1, LANE), lambda b, g: (0, g))],
                  out_specs=[blk, blk, blk],
                  out_shape=[_sds((T, POOL_W), MX), _sds((T, POOL_W), F32), _sds((T, POOL_W), MX)])(z, pool_grp, pool_scale)


def _pool_bwd(dscaled, mixed, pooled, pool_grp, pool_scale, *, nb, name):
    T = dscaled.shape[0]
    S = T // nb
    blk = pl.BlockSpec((S, LANE), lambda g, b: (b, g))

    def body(ds_ref, mixed_ref, pooled_ref, w_ref, s_ref, du_ref, dw_ref, dsc_ref):
        g = pl.program_id(0)
        b = pl.program_id(1)
        ds = ds_ref[...]
        dsc_ref[0] = jnp.sum(ds * mixed_ref[...], axis=0, keepdims=True)
        dmixed = (ds * s_ref[...]).astype(MX)
        dw = lax.dot_general(pooled_ref[...], dmixed, (((0,), (0,)), ((), ())), preferred_element_type=F32)

        @pl.when(b == 0)
        def _():
            dw_ref[0] = dw

        @pl.when(b > 0)
        def _():
            dw_ref[0] += dw
        dpooled = lax.dot_general(dmixed, w_ref[0], (((1,), (1,)), ((), ())), preferred_element_type=F32)
        row = lax.broadcasted_iota(jnp.int32, dpooled.shape, 0)
        width = lax.shift_left(jnp.int32(2), g)
        q = dpooled / jnp.minimum(row + 1, width).astype(F32)
        r2 = q + _shift_up(q, 1, row, S)
        r4 = r2 + _shift_up(r2, 2, row, S)
        r8 = r4 + _shift_up(r4, 4, row, S)
        r16 = r8 + _shift_up(r8, 8, row, S)
        win = jnp.where(g == 0, r2, jnp.where(g == 1, r4, jnp.where(g == 2, r8, r16)))
        du_ref[...] = (win - dpooled).astype(du_ref.dtype)

    return _pcall(body, name=name, grid=(POOL_G, nb),
                  in_specs=[blk, blk, blk, pl.BlockSpec((1, LANE, LANE), lambda g, b: (g, 0, 0)),
                            pl.BlockSpec((1, LANE), lambda g, b: (0, g))],
                  out_specs=[blk, pl.BlockSpec((1, LANE, LANE), lambda g, b: (g, 0, 0)),
                             pl.BlockSpec((1, 1, LANE), lambda g, b: (b, 0, g))],
                  out_shape=[_sds((T, POOL_W), MX), _sds((POOL_G, LANE, LANE), F32), _sds((nb, 1, POOL_W), F32)],
                  )(dscaled, mixed, pooled, pool_grp, pool_scale)


def _lat_fwd(z, q_gain, kv_gain, *, name, tm=256):
    T = z.shape[0]

    def body(z_ref, qg_ref, kg_ref, qn_ref, kvn_ref):
        ql = z_ref[:, Z_QL:Z_QL + QL]
        kv = z_ref[:, Z_KV:Z_KV + KVL]
        qn_ref[...] = (ql * _rsq(ql) * qg_ref[...]).astype(qn_ref.dtype)
        kvn_ref[...] = (kv * _rsq(kv) * kg_ref[...]).astype(kvn_ref.dtype)

    return _pcall(body, name=name, grid=(T // tm,),
                  in_specs=[_row_spec(tm, LAT_W), pl.BlockSpec((1, QL), lambda i: (0, 0)), pl.BlockSpec((1, KVL), lambda i: (0, 0))],
                  out_specs=[_row_spec(tm, QL), _row_spec(tm, KVL)],
                  out_shape=[_sds((T, QL), MX), _sds((T, KVL), MX)])(z, q_gain, kv_gain)


def _lat_bwd(z, dqn, dkvn, q_gain, kv_gain, *, nb, name, tm=256):
    T = z.shape[0]
    tps = (T // nb) // tm

    def norm_bwd(x, dy, gain):
        r = _rsq(x)
        xhat = x * r
        dgain = jnp.sum(dy * xhat, axis=0, keepdims=True)
        dxhat = dy * gain
        return r * (dxhat - xhat * jnp.mean(dxhat * xhat, axis=-1, keepdims=True)), dgain

    def body(z_ref, dq_ref, dkv_ref, qg_ref, kg_ref, dql_ref, dkvl_ref, sq_ref, sk_ref):
        i = pl.program_id(0)
        dql, dqg = norm_bwd(z_ref[:, Z_QL:Z_QL + QL], dq_ref[...], qg_ref[...])
        dkvl, dkg = norm_bwd(z_ref[:, Z_KV:Z_KV + KVL], dkv_ref[...], kg_ref[...])
        dql_ref[...] = dql.astype(dql_ref.dtype)
        dkvl_ref[...] = dkvl.astype(dkvl_ref.dtype)

        @pl.when(i % tps == 0)
        def _():
            sq_ref[...] = jnp.zeros_like(sq_ref)
            sk_ref[...] = jnp.zeros_like(sk_ref)

        sq_ref[0, 0:1, :] += dqg
        sk_ref[0, 0:1, :] += dkg

    return _pcall(body, name=name, grid=(T // tm,),
                  in_specs=[_row_spec(tm, LAT_W), _row_spec(tm, QL), _row_spec(tm, KVL),
                            pl.BlockSpec((1, QL), lambda i: (0, 0)), pl.BlockSpec((1, KVL), lambda i: (0, 0))],
                  out_specs=[_row_spec(tm, QL), _row_spec(tm, KVL),
                             pl.BlockSpec((1, SUBLANE, QL), lambda i: (i // tps, 0, 0)),
                             pl.BlockSpec((1, SUBLANE, KVL), lambda i: (i // tps, 0, 0))],
                  out_shape=[_sds((T, QL), MX), _sds((T, KVL), MX), _sds((nb, SUBLANE, QL), F32), _sds((nb, SUBLANE, KVL), F32)],
                  )(z, dqn, dkvn, q_gain, kv_gain)


def _lane_masks(shape):
    lane = lax.broadcasted_iota(jnp.int32, shape, len(shape) - 1)
    m_n = lane < NOPE
    m_r = jnp.logical_and(lane >= KR_LANE, lane < KR_LANE + ROPE)
    first_half = lane < KR_LANE + ROPE // 2
    return m_n, m_r, first_half


def _rot(y, first_half):
    return jnp.where(first_half, -pltpu.roll(y, LANE - ROPE // 2, 1), pltpu.roll(y, ROPE // 2, 1))


def _rot_t(v, first_half, m_r):
    return jnp.where(m_r, jnp.where(first_half, pltpu.roll(v, LANE - ROPE // 2, 1), -pltpu.roll(v, ROPE // 2, 1)), 0.0)


def _prep_fwd(qp, kvp, z, cos, sin, q_gain, k_gain, *, name, tm=256):
    T = qp.shape[0]
    slab = pl.BlockSpec((tm, LANE), lambda i: (i, 0))
    kr_spec = pl.BlockSpec((tm, LANE), lambda i: (i, Z_KR // LANE))
    vec = pl.BlockSpec((1, LANE), lambda i: (0, 0))

    def body(qp_ref, kvp_ref, kr_ref, cos_ref, sin_ref, qg_ref, kg_ref, q_ref, k_ref, v_ref):
        m_n, m_r, first_half = _lane_masks((tm, LANE))
        c = cos_ref[...]
        s = sin_ref[...]
        qg = qg_ref[...]
        kg = kg_ref[...]
        xr = kr_ref[...]
        rr = lax.rsqrt(jnp.sum(xr * xr, axis=-1, keepdims=True) * (1.0 / ROPE) + EPS)
        yr = xr * rr * kg
        kr = jnp.where(m_r, yr * c + _rot(yr, first_half) * s, 0.0)
        for h in range(NH):
            x = qp_ref[:, h * LANE:(h + 1) * LANE]
            x2 = x * x
            rn = lax.rsqrt(jnp.sum(jnp.where(m_n, x2, 0.0), axis=-1, keepdims=True) * (1.0 / NOPE) + EPS)
            rq = lax.rsqrt(jnp.sum(jnp.where(m_r, x2, 0.0), axis=-1, keepdims=True) * (1.0 / ROPE) + EPS)
            y = x * jnp.where(m_n, rn, jnp.where(m_r, rq, 0.0)) * qg
            q_ref[:, h * LANE:(h + 1) * LANE] = (y * c + _rot(y, first_half) * s).astype(q_ref.dtype)
            xk = kvp_ref[:, h * LANE:(h + 1) * LANE]
            rk = lax.rsqrt(jnp.sum(jnp.where(m_n, xk * xk, 0.0), axis=-1, keepdims=True) * (1.0 / NOPE) + EPS)
            k_ref[:, h * LANE:(h + 1) * LANE] = (jnp.where(m_n, xk * rk * kg, 0.0) + kr).astype(k_ref.dtype)
        v_ref[...] = kvp_ref[:, NH * LANE:].astype(v_ref.dtype)

    return _pcall(body, name=name, grid=(T // tm,),
                  in_specs=[_row_spec(tm, NH * LANE), _row_spec(tm, NH * LANE + NH * NOPE), kr_spec, slab, slab, vec, vec],
                  out_specs=[_row_spec(tm, NH * LANE), _row_spec(tm, NH * LANE), _row_spec(tm, NH * NOPE)],
                  out_shape=[_sds((T, NH * LANE), MX), _sds((T, NH * LANE), MX), _sds((T, NH * NOPE), MX)],
                  )(qp, kvp, z, cos, sin, q_gain, k_gain)


def _prep_bwd(dq, dk, dv, qp, kvp, z, cos, sin, q_gain, k_gain, *, nb, name, tm=256):
    T = qp.shape[0]
    tps = (T // nb) // tm
    slab = pl.BlockSpec((tm, LANE), lambda i: (i, 0))
    kr_spec = pl.BlockSpec((tm, LANE), lambda i: (i, Z_KR // LANE))
    vec = pl.BlockSpec((1, LANE), lambda i: (0, 0))

    def body(dq_ref, dk_ref, dv_ref, qp_ref, kvp_ref, kr_ref, cos_ref, sin_ref, qg_ref, kg_ref,
             dqp_ref, dkvp_ref, dkr_ref, st_ref):
        i = pl.program_id(0)
        m_n, m_r, first_half = _lane_masks((tm, LANE))
        c = cos_ref[...]
        s = sin_ref[...]
        qg = qg_ref[...]
        kg = kg_ref[...]
        dqg = jnp.zeros((1, LANE), F32)
        dkg = jnp.zeros((1, LANE), F32)
        dkr_sum = jnp.zeros((tm, LANE), F32)
        for h in range(NH):
            x = qp_ref[:, h * LANE:(h + 1) * LANE]
            x2 = x * x
            rn = lax.rsqrt(jnp.sum(jnp.where(m_n, x2, 0.0), axis=-1, keepdims=True) * (1.0 / NOPE) + EPS)
            rq = lax.rsqrt(jnp.sum(jnp.where(m_r, x2, 0.0), axis=-1, keepdims=True) * (1.0 / ROPE) + EPS)
            rfac = jnp.where(m_n, rn, jnp.where(m_r, rq, 0.0))
            xhat = x * rfac
            do = dq_ref[:, h * LANE:(h + 1) * LANE]
            dy = do * c + _rot_t(do * s, first_half, m_r)
            dqg = dqg + jnp.sum(dy * xhat, axis=0, keepdims=True)
            dxhat = dy * qg
            t = dxhat * xhat
            mean_n = jnp.sum(jnp.where(m_n, t, 0.0), axis=-1, keepdims=True) * (1.0 / NOPE)
            mean_r = jnp.sum(jnp.where(m_r, t, 0.0), axis=-1, keepdims=True) * (1.0 / ROPE)
            dqp_ref[:, h * LANE:(h + 1) * LANE] = (
                rfac * (dxhat - xhat * jnp.where(m_n, mean_n, jnp.where(m_r, mean_r, 0.0)))).astype(dqp_ref.dtype)

            xk = kvp_ref[:, h * LANE:(h + 1) * LANE]
            rk = lax.rsqrt(jnp.sum(jnp.where(m_n, xk * xk, 0.0), axis=-1, keepdims=True) * (1.0 / NOPE) + EPS)
            khat = jnp.where(m_n, xk * rk, 0.0)
            dko = dk_ref[:, h * LANE:(h + 1) * LANE]
            dkn = jnp.where(m_n, dko, 0.0)
            dkg = dkg + jnp.sum(dkn * khat, axis=0, keepdims=True)
            dkhat = dkn * kg
            mean_k = jnp.sum(dkhat * khat, axis=-1, keepdims=True) * (1.0 / NOPE)
            dkvp_ref[:, h * LANE:(h + 1) * LANE] = jnp.where(m_n, rk * (dkhat - khat * mean_k), 0.0).astype(dkvp_ref.dtype)
            dkr_sum = dkr_sum + jnp.where(m_r, dko, 0.0)
        dkvp_ref[:, NH * LANE:] = dv_ref[...].astype(dkvp_ref.dtype)

        xr = kr_ref[...]
        rr = lax.rsqrt(jnp.sum(xr * xr, axis=-1, keepdims=True) * (1.0 / ROPE) + EPS)
        rhat = xr * rr
        dyr = dkr_sum * c + _rot_t(dkr_sum * s, first_half, m_r)
        dkg = dkg + jnp.sum(dyr * rhat, axis=0, keepdims=True)
        drhat = dyr * kg
        mean_kr = jnp.sum(drhat * rhat, axis=-1, keepdims=True) * (1.0 / ROPE)
        dkr_ref[...] = jnp.where(m_r, rr * (drhat - rhat * mean_kr), 0.0).astype(dkr_ref.dtype)

        @pl.when(i % tps == 0)
        def _():
            st_ref[...] = jnp.zeros_like(st_ref)

        st_ref[0, 0:1, :] += dqg
        st_ref[0, 1:2, :] += dkg

    return _pcall(body, name=name, grid=(T // tm,),
                  in_specs=[_row_spec(tm, NH * LANE), _row_spec(tm, NH * LANE), _row_spec(tm, NH * NOPE),
                            _row_spec(tm, NH * LANE), _row_spec(tm, NH * LANE + NH * NOPE), kr_spec, slab, slab, vec, vec],
                  out_specs=[_row_spec(tm, NH * LANE), _row_spec(tm, NH * LANE + NH * NOPE), slab,
                             pl.BlockSpec((1, SUBLANE, LANE), lambda i: (i // tps, 0, 0))],
                  out_shape=[_sds((T, NH * LANE), MX), _sds((T, NH * LANE + NH * NOPE), MX), _sds((T, LANE), MX),
                             _sds((nb, SUBLANE, LANE), F32)],
                  )(dq, dk, dv, qp, kvp, z, cos, sin, q_gain, k_gain)


def _lower_triangle(t):
    return lax.broadcasted_iota(jnp.int32, (t, t), 1) <= lax.broadcasted_iota(jnp.int32, (t, t), 0)


def _attn_fwd(q, k, v, *, nb, name, tq=512, comm=None):
    T = q.shape[0]
    S = T // nb
    tq = _pick(S, tq)
    nq = S // tq
    tk = tq
    npair = NH // 2

    def body(q_ref, k_ref, v_ref, o_ref, lse_ref):
        qi = pl.program_id(2)
        lane = lax.broadcasted_iota(jnp.int32, (tq, LANE), 1)
        qs = [q_ref[:, hh * LANE:(hh + 1) * LANE] for hh in range(2)]

        def block(j, carry, diagonal):
            k0 = pl.multiple_of(j * tk, tk)
            vb = v_ref[pl.ds(k0, tk), :]
            new = []
            for hh in range(2):
                m, l, acc = carry[hh]
                kb = k_ref[pl.ds(k0, tk), hh * LANE:(hh + 1) * LANE]
                s = lax.dot_general(qs[hh], kb, (((1,), (1,)), ((), ())), preferred_element_type=F32) * ATTN_SCALE
                if diagonal:
                    s = jnp.where(_lower_triangle(tq), s, NEG)
                m_new = jnp.maximum(m, jnp.max(s, axis=-1, keepdims=True))
                p = jnp.exp(s - m_new)
                alpha = jnp.exp(m - m_new)
                l = alpha * l + jnp.sum(p, axis=-1, keepdims=True)
                acc = alpha * acc + jnp.dot(p.astype(MX), vb, preferred_element_type=F32)
                new.append((m_new, l, acc))
            return tuple(new)

        init = tuple((jnp.full((tq, 1), NEG, F32), jnp.zeros((tq, 1), F32), jnp.zeros((tq, LANE), F32)) for _ in range(2))
        carry = lax.fori_loop(0, qi, lambda j, c: block(j, c, False), init)
        (m0, l0, acc0), (m1, l1, acc1) = block(qi, carry, True)
        o_ref[...] = jnp.where(lane < NOPE, acc0 / l0, acc1 / l1).astype(o_ref.dtype)
        lse_ref[...] = jnp.where(lane < NOPE, m0 + jnp.log(l0), m1 + jnp.log(l1))

    return _pcall(body, name=name, grid=(nb, npair, nq),
                  in_specs=[pl.BlockSpec((tq, 2 * LANE), lambda b, p, i: (b * nq + i, p)),
                            pl.BlockSpec((S, 2 * LANE), lambda b, p, i: (b, p)),
                            pl.BlockSpec((S, LANE), lambda b, p, i: (b, p))],
                  out_specs=[pl.BlockSpec((tq, LANE), lambda b, p, i: (b * nq + i, p)),
                             pl.BlockSpec((tq, LANE), lambda b, p, i: (b * nq + i, p))],
                  out_shape=[_sds((T, NH * NOPE), MX), _sds((T, NH * NOPE), F32)], comm=comm)(q, k, v)


def _attn_bwd(q, k, v, o, lse, do, *, nb, name, tq=512, comm=None):
    T = q.shape[0]
    S = T // nb
    tq = _pick(S, tq)
    nq = S // tq
    tk = tq
    npair = NH // 2

    def body(q_ref, k_ref, v_ref, o_ref, lse_ref, do_ref, dq_ref, dk_ref, dv_ref, delta_ref):
        lane = lax.broadcasted_iota(jnp.int32, (tq, LANE), 1)
        first = lane < NOPE
        dq_ref[...] = jnp.zeros_like(dq_ref)

        def delta_step(qi, _):
            q0 = pl.multiple_of(qi * tq, tq)
            prod = do_ref[pl.ds(q0, tq), :] * o_ref[pl.ds(q0, tq), :].astype(F32)
            d0 = jnp.sum(jnp.where(first, prod, 0.0), axis=-1, keepdims=True)
            d1 = jnp.sum(jnp.where(first, 0.0, prod), axis=-1, keepdims=True)
            delta_ref[pl.ds(q0, tq), :] = jnp.where(first, d0, d1)
            return 0

        lax.fori_loop(0, nq, delta_step, 0)

        def kv_step(kj, _):
            k0 = pl.multiple_of(kj * tk, tk)
            kbs = [k_ref[pl.ds(k0, tk), hh * LANE:(hh + 1) * LANE] for hh in range(2)]
            vb = v_ref[pl.ds(k0, tk), :]

            def q_block(qi, carry, diagonal):
                dk0, dk1, dv = carry
                dks = [dk0, dk1]
                q0 = pl.multiple_of(qi * tq, tq)
                dov = do_ref[pl.ds(q0, tq), :]
                lse_v = lse_ref[pl.ds(q0, tq), :]
                delta_v = delta_ref[pl.ds(q0, tq), :]
                for hh in range(2):
                    qh = q_ref[pl.ds(q0, tq), hh * LANE:(hh + 1) * LANE]
                    dob = jnp.where(first if hh == 0 else jnp.logical_not(first), dov, 0.0).astype(MX)
                    s = lax.dot_general(qh, kbs[hh], (((1,), (1,)), ((), ())), preferred_element_type=F32) * ATTN_SCALE
                    p = jnp.exp(s - lse_v[:, hh * NOPE:hh * NOPE + 1])
                    if diagonal:
                        p = jnp.where(_lower_triangle(tq), p, 0.0)
                    dp = lax.dot_general(dob, vb, (((1,), (1,)), ((), ())), preferred_element_type=F32)
                    ds = (p * (dp - delta_v[:, hh * NOPE:hh * NOPE + 1]) * ATTN_SCALE).astype(MX)
                    dks[hh] = dks[hh] + lax.dot_general(ds, qh, (((0,), (0,)), ((), ())), preferred_element_type=F32)
                    dv = dv + lax.dot_general(p.astype(MX), dob, (((0,), (0,)), ((), ())), preferred_element_type=F32)
                    dq_ref[pl.ds(q0, tq), hh * LANE:(hh + 1) * LANE] += jnp.dot(ds, kbs[hh], preferred_element_type=F32)
                return dks[0], dks[1], dv

            zero = jnp.zeros((tk, LANE), F32)
            carry = q_block(kj, (zero, zero, zero), True)
            dk0, dk1, dv = lax.fori_loop(kj + 1, nq, lambda qi, c: q_block(qi, c, False), carry)
            dk_ref[pl.ds(k0, tk), 0:LANE] = dk0
            dk_ref[pl.ds(k0, tk), LANE:2 * LANE] = dk1
            dv_ref[pl.ds(k0, tk), :] = dv
            return 0

        lax.fori_loop(0, nq, kv_step, 0)

    pair256 = pl.BlockSpec((S, 2 * LANE), lambda b, p: (b, p))
    pair128 = pl.BlockSpec((S, LANE), lambda b, p: (b, p))
    return _pcall(body, name=name, grid=(nb, npair),
                  in_specs=[pair256, pair256, pair128, pair128, pair128, pair128],
                  out_specs=[pair256, pair256, pair128],
                  out_shape=[_sds((T, NH * LANE), F32), _sds((T, NH * LANE), F32), _sds((T, NH * NOPE), F32)],
                  scratch=[pltpu.VMEM((S, LANE), F32)], comm=comm)(q, k, v, o, lse, do)


class _NoExchange:
    def __init__(self, weights):
        self.W, self.G = weights, {}

    def rider(self, name):
        return None

    def landed(self, name, outs):
        pass

    def grad(self, key, g):
        self.G[key] = g


def _run(plan, fn, *args, name, **kw):
    rider = plan.rider(name)
    if rider is None:
        return fn(*args, name=name, **kw)
    outs, landed = fn(*args, name=name, comm=rider, **kw)
    plan.landed(name, landed)
    return outs


def _layer_fwd_bwd(x, tgt, mod3, cos, sin, plan, P):
    nb = mod3.shape[0]
    W = plan.W
    h1, gu1, a1 = _run(plan, _ffn_up_normed, x, mod3, P["norm_ffn1"], W["ffn1_in"], sub=0, name="ffn1_up")
    f1 = _mm(a1, W["ffn1_out"], mode="nn", out_dtype=F32, name="ffn1_out", tm=1024, tn=1024, tk=DFF)
    x1, h2 = _fwd_block(x, f1, mod3, P["norm_mix"], sub=1, coef=0.5, name="fwd_norm2")
    z = _run(plan, _mm, h2, W["w_in"], mode="nn", out_dtype=F32, name="mix_in", tn=1664)
    pooled, mixed, scaled = _pool_fwd(z, W["pool_grp"], P["pool_scale"], nb=nb, name="pool_fwd")
    br_pool = _mm(scaled, W["pool_proj"], mode="nn", out_dtype=F32, name="pool_proj")
    qn, kvn = _lat_fwd(z, P["q_a_norm"], P["kv_a_norm"], name="lat_fwd")
    qp = _mm(qn, W["q_up"], mode="nn", out_dtype=F32, name="q_up")
    kvp = _mm(kvn, W["kv_up"], mode="nn", out_dtype=F32, name="kv_up")
    q, k, v = _prep_fwd(qp, kvp, z, cos, sin, P["q_gain"], P["k_gain"], name="prep_fwd")
    attn, lse = _run(plan, _attn_fwd, q, k, v, nb=nb, name="attn_fwd")
    br_mla = _mm(attn, W["mla_proj"], mode="nn", out_dtype=F32, name="mla_proj")
    merged = _merge_fwd(z, br_pool, br_mla, name="merge_fwd")
    mo = _mm(merged, W["w_out"], mode="nn", out_dtype=F32, name="mix_out")
    x2, h3 = _fwd_block(x1, mo, mod3, P["norm_ffn2"], sub=2, coef=1.0, name="fwd_norm3")
    gu2, a2 = _ffn_up(h3, W["ffn2_in"], name="ffn2_up")
    f2 = _mm(a2, W["ffn2_out"], mode="nn", out_dtype=F32, name="ffn2_out", tm=1024, tn=1024, tk=DFF)
    dy, df2, st_fin, loss = _final(x2, f2, tgt, mod3, name="loss_head")

    plan.grad("ffn2_out", _dw(a2, df2, name="d_ffn2_out"))
    dgu2 = _ffn_dact(df2, gu2, W["ffn2_out"], name="ffn2_dact")
    dh3 = _ffn_dh(dgu2, W["ffn2_in"], name="d_ffn2_h")
    plan.grad("ffn2_in", _ffn_dw_in(h3, dgu2, name="d_ffn2_in"))
    dx2, dmo, st3 = _run(plan, _bwd_block, x2, dh3, dy, mo, mod3, P["norm_ffn2"], sub=2, coef=1.0, name="bwd_norm3")
    plan.grad("w_out", _dw(merged, dmo, name="d_mix_out"))
    dmerged = _mm(dmo, W["w_out"], mode="nt", out_dtype=F32, name="d_merged")
    dbr_pool, dbr_mla, dgates = _merge_bwd(z, br_pool, br_mla, dmerged, name="merge_bwd")
    plan.grad("pool_proj", _dw(scaled, dbr_pool, name="d_pool_proj"))
    dscaled = _mm(dbr_pool, W["pool_proj"], mode="nt", out_dtype=F32, name="d_pool_scaled")
    du_pool, d_pool_grp, d_pool_scale = _pool_bwd(dscaled, mixed, pooled, W["pool_grp"], P["pool_scale"], nb=nb, name="pool_bwd")
    plan.grad("mla_proj", _dw(attn, dbr_mla, name="d_mla_proj"))
    dattn = _mm(dbr_mla, W["mla_proj"], mode="nt", out_dtype=F32, name="d_attn")
    dq, dk, dv = _run(plan, _attn_bwd, q, k, v, attn, lse, dattn, nb=nb, name="attn_bwd")
    dqp, dkvp, dkr, st_prep = _prep_bwd(dq, dk, dv, qp, kvp, z, cos, sin, P["q_gain"], P["k_gain"], nb=nb, name="prep_bwd")
    plan.grad("q_up", _dw(qn, dqp, name="d_q_up"))
    dqn = _mm(dqp, W["q_up"], mode="nt", out_dtype=F32, name="d_qn")
    plan.grad("kv_up", _dw(kvn, dkvp, name="d_kv_up"))
    dkvn = _mm(dkvp, W["kv_up"], mode="nt", out_dtype=F32, name="d_kvn")
    dql, dkvl, st_q, st_kv = _lat_bwd(z, dqn, dkvn, P["q_a_norm"], P["kv_a_norm"], nb=nb, name="lat_bwd")
    dz = jnp.concatenate([du_pool, dql, dkvl, dkr, dgates], axis=1)
    plan.grad("w_in", _dw(h2, dz, name="d_mix_in", tn=1664))
    dh2 = _run(plan, _mm, dz, W["w_in"], mode="nt", out_dtype=F32, name="d_mix_h")
    dx1, df1, st2 = _run(plan, _bwd_block, x1, dh2, dx2, f1, mod3, P["norm_mix"], sub=1, coef=0.5, name="bwd_norm2")
    plan.grad("ffn1_out", _run(plan, _dw, a1, df1, name="d_ffn1_out"))
    dgu1 = _run(plan, _ffn_dact, df1, gu1, W["ffn1_out"], name="ffn1_dact")
    plan.grad("ffn1_in", _run(plan, _ffn_dw_in, h1, dgu1, name="d_ffn1_in"))
    dh1 = _run(plan, _ffn_dh, dgu1, W["ffn1_in"], name="d_ffn1_h")
    grad_x, st1 = _run(plan, _bwd_block, x, dh1, dx1, None, mod3, P["norm_ffn1"], sub=0, coef=0.0, name="bwd_norm1")

    return loss, grad_x, (st1, st2, st3, st_fin, st_q, st_kv, st_prep, d_pool_scale), d_pool_grp


def _w_in_to_kernel(w):
    k = w.shape[0]
    zeros = lambda n: jnp.zeros((k, n), w.dtype)
    return jnp.concatenate([w[:, 0:1152], zeros(KR_LANE), w[:, 1152:1184], zeros(LANE - KR_LANE - ROPE), w[:, 1184:]], axis=1)


def _w_in_from_kernel(g):
    return jnp.concatenate([g[:, 0:1152], g[:, Z_KR + KR_LANE:Z_KR + KR_LANE + ROPE], g[:, Z_GP:]], axis=1)


def _q_up_to_kernel(w):
    k = w.shape[0]
    return jnp.pad(w.reshape(k, NH, NOPE + ROPE), ((0, 0), (0, 0), (0, LANE - NOPE - ROPE))).reshape(k, NH * LANE)


def _q_up_from_kernel(g):
    k = g.shape[0]
    return g.reshape(k, NH, LANE)[:, :, :NOPE + ROPE].reshape(k, NH * (NOPE + ROPE))


def _kv_up_to_kernel(w):
    k = w.shape[0]
    w3 = w.reshape(k, NH, 2 * NOPE)
    kpart = jnp.pad(w3[:, :, :NOPE], ((0, 0), (0, 0), (0, LANE - NOPE))).reshape(k, NH * LANE)
    return jnp.concatenate([kpart, w3[:, :, NOPE:].reshape(k, NH * NOPE)], axis=1)


def _kv_up_from_kernel(g):
    k = g.shape[0]
    kpart = g[:, :NH * LANE].reshape(k, NH, LANE)[:, :, :NOPE]
    vpart = g[:, NH * LANE:].reshape(k, NH, NOPE)
    return jnp.concatenate([kpart, vpart], axis=2).reshape(k, NH * 2 * NOPE)


def _gain_slab(nope, rope):
    return jnp.concatenate([nope, rope, jnp.zeros((1, LANE - NOPE - ROPE), nope.dtype)], axis=1)


def _rope_tables(positions):
    inv_freq = 10000.0 ** (-jnp.arange(0, ROPE, 2, dtype=F32) / ROPE)
    ang = positions.astype(F32)[:, None] * inv_freq
    ang = jnp.concatenate([ang, ang], axis=-1)
    t = positions.shape[0]
    cos = jnp.concatenate([jnp.ones((t, KR_LANE), F32), jnp.cos(ang), jnp.ones((t, LANE - KR_LANE - ROPE), F32)], axis=1)
    sin = jnp.concatenate([jnp.zeros((t, KR_LANE), F32), jnp.sin(ang), jnp.zeros((t, LANE - KR_LANE - ROPE), F32)], axis=1)
    return cos, sin


def _coords():
    return lax.axis_index("x"), lax.axis_index("y"), lax.axis_index("c")


HBM_SPEC = pl.BlockSpec(memory_space=pl.ANY)
VMEM_SPEC = pl.BlockSpec(memory_space=pltpu.VMEM)


def _gather8(v, *, name, comm=None):
    r, c = v.shape

    def body(v_ref, out_ref, send_sems, recv_sems, local_sem):
        x, y, cc = _coords()
        me = 4 * x + 2 * y + cc
        mine = pltpu.make_async_copy(v_ref, out_ref.at[me], local_sem)
        mine.start()
        copies = []
        for kk in range(1, N_DEV):
            peer = (x ^ (kk >> 2), y ^ ((kk >> 1) & 1), cc ^ (kk & 1))
            cp = pltpu.make_async_remote_copy(src_ref=v_ref, dst_ref=out_ref.at[me], send_sem=send_sems.at[kk - 1],
                                              recv_sem=recv_sems.at[kk - 1], device_id=peer, device_id_type=MESH)
            cp.start()
            copies.append(cp)
        for kk in range(1, N_DEV):
            peer_slot = me ^ kk
            pltpu.make_async_remote_copy(src_ref=v_ref, dst_ref=out_ref.at[peer_slot], send_sem=send_sems.at[kk - 1],
                                         recv_sem=recv_sems.at[kk - 1], device_id=(x, y, cc), device_id_type=MESH).wait_recv()
        for cp in copies:
            cp.wait_send()
        mine.wait()

    return _pcall(body, name=name, out_shape=_sds((N_DEV, r, c), v.dtype), in_specs=[VMEM_SPEC], out_specs=VMEM_SPEC,
                  scratch=[pltpu.SemaphoreType.DMA((N_DEV - 1,)), pltpu.SemaphoreType.DMA((N_DEV - 1,)), pltpu.SemaphoreType.DMA],
                  comm=comm)(v)


CHIP_RELS = ((1, 0), (0, 1), (1, 1))


def _comm_only(comm, *, name):
    _, outs = _pcall(lambda: None, name=name, out_shape=[], in_specs=[], out_specs=[], comm=comm)()
    return outs


def _gather_comm(shards):
    n = len(shards)

    def ici(ins, outs, sems, a, j, x, y, cc):
        half = ins[a].shape[0] // 2
        mine = pl.ds(cc * half, half)
        dx, dy = CHIP_RELS[j]
        return pltpu.make_async_remote_copy(src_ref=ins[a].at[mine], dst_ref=outs[a].at[2 * x + y, mine],
                                            send_sem=sems[0].at[a, j], recv_sem=sems[1].at[a, j],
                                            device_id=(x ^ dx, y ^ dy, cc), device_id_type=MESH)

    def d2d(ins, outs, sems, a, j, x, y, cc, half_of):
        half = ins[a].shape[0] // 2
        dx, dy = CHIP_RELS[j]
        landed = outs[a].at[2 * (x ^ dx) + (y ^ dy), pl.ds(half_of * half, half)]
        return pltpu.make_async_remote_copy(src_ref=landed, dst_ref=landed, send_sem=sems[2].at[a, j], recv_sem=sems[3].at[a, j],
                                            device_id=(x, y, 1 - cc), device_id_type=MESH)

    def own(ins, outs, sems, a, x, y):
        return pltpu.make_async_copy(ins[a], outs[a].at[2 * x + y], sems[4].at[a])

    def start(ins, outs, sems):
        x, y, cc = _coords()
        for a in range(n):
            own(ins, outs, sems, a, x, y).start()
            for j in range(3):
                ici(ins, outs, sems, a, j, x, y, cc).start()

    def finish(ins, outs, sems):
        x, y, cc = _coords()
        for a in range(n):
            for j in range(3):
                ici(ins, outs, sems, a, j, x, y, cc).wait_recv()
                d2d(ins, outs, sems, a, j, x, y, cc, cc).start()
        for a in range(n):
            for j in range(3):
                d2d(ins, outs, sems, a, j, x, y, cc, 1 - cc).wait_recv()
        for a in range(n):
            for j in range(3):
                ici(ins, outs, sems, a, j, x, y, cc).wait_send()
                d2d(ins, outs, sems, a, j, x, y, cc, cc).wait_send()
            own(ins, outs, sems, a, x, y).wait()

    dma = pltpu.SemaphoreType.DMA
    return _Comm(shards, [_sds((N_CHIP,) + s.shape, s.dtype) for s in shards],
                 [dma((n, 3)), dma((n, 3)), dma((n, 3)), dma((n, 3)), dma((n,))], start, finish)


def _swap_comm(parts):
    n = len(parts)

    def copy(ins, outs, sems, a):
        x, y, cc = _coords()
        half = ins[a].shape[1] // 2
        return pltpu.make_async_remote_copy(src_ref=ins[a].at[:, pl.ds((1 - cc) * half, half)], dst_ref=outs[a],
                                            send_sem=sems[0].at[a], recv_sem=sems[1].at[a], device_id=(x, y, 1 - cc),
                                            device_id_type=MESH)

    def start(ins, outs, sems):
        for a in range(n):
            copy(ins, outs, sems, a).start()

    def finish(ins, outs, sems):
        for a in range(n):
            copy(ins, outs, sems, a).wait()

    dma = pltpu.SemaphoreType.DMA
    return _Comm(parts, [_sds((p.shape[0], p.shape[1] // 2, p.shape[2]), p.dtype) for p in parts], [dma((n,)), dma((n,))],
                 start, finish)


def _add_half(full, other, cidx, *, name):
    nch, r, c = full.shape
    half = r // 2
    tr = _pick_rows(half)
    nbk = half // tr
    grid_spec = pltpu.PrefetchScalarGridSpec(
        num_scalar_prefetch=1, grid=(nch, nbk),
        in_specs=[pl.BlockSpec((1, tr, c), lambda j, i, cref: (j, cref[0] * nbk + i, 0)),
                  pl.BlockSpec((1, tr, c), lambda j, i, cref: (j, i, 0))],
        out_specs=pl.BlockSpec((1, tr, c), lambda j, i, cref: (j, i, 0)))

    def body(cref, a_ref, b_ref, o_ref):
        o_ref[...] = (a_ref[...] + b_ref[...]).astype(o_ref.dtype)

    return _pcall(body, name=name, out_shape=_sds((nch, half, c), MX), grid_spec=grid_spec)(cidx, full, other)


def _pick_rows(rows, target=512):
    best = None
    for t in range(16, min(rows, target) + 1, 16):
        if rows % t == 0:
            best = t
    return rows if best is None else best


def _exchange_comm(parts):
    n = len(parts)

    def send(ins, outs, sems, a, j, x, y, cc):
        dx, dy = CHIP_RELS[j]
        return pltpu.make_async_remote_copy(src_ref=ins[a].at[2 * (x ^ dx) + (y ^ dy)], dst_ref=outs[a].at[2 * x + y],
                                            send_sem=sems[0].at[a, j], recv_sem=sems[1].at[a, j],
                                            device_id=(x ^ dx, y ^ dy, cc), device_id_type=MESH)

    def landing(ins, outs, sems, a, j, x, y, cc):
        dx, dy = CHIP_RELS[j]
        peer_chip = 2 * (x ^ dx) + (y ^ dy)
        return pltpu.make_async_remote_copy(src_ref=ins[a].at[peer_chip], dst_ref=outs[a].at[peer_chip], send_sem=sems[0].at[a, j],
                                            recv_sem=sems[1].at[a, j], device_id=(x, y, cc), device_id_type=MESH)

    def own(ins, outs, sems, a, x, y):
        return pltpu.make_async_copy(ins[a].at[2 * x + y], outs[a].at[2 * x + y], sems[2].at[a])

    def start(ins, outs, sems):
        x, y, cc = _coords()
        for a in range(n):
            own(ins, outs, sems, a, x, y).start()
            for j in range(3):
                send(ins, outs, sems, a, j, x, y, cc).start()

    def finish(ins, outs, sems):
        x, y, cc = _coords()
        for a in range(n):
            for j in range(3):
                landing(ins, outs, sems, a, j, x, y, cc).wait_recv()
        for a in range(n):
            for j in range(3):
                send(ins, outs, sems, a, j, x, y, cc).wait_send()
            own(ins, outs, sems, a, x, y).wait()

    dma = pltpu.SemaphoreType.DMA
    return _Comm(parts, [_sds(p.shape, p.dtype) for p in parts], [dma((n, 3)), dma((n, 3)), dma((n,))], start, finish)


def _sum_chips(q, cidx, *, name):
    nch, h, c = q.shape
    tr = _pick_rows(h)
    nbk = h // tr
    grid_spec = pltpu.PrefetchScalarGridSpec(
        num_scalar_prefetch=1, grid=(nbk,),
        in_specs=[pl.BlockSpec((nch, tr, c), lambda i, cref: (0, i, 0))],
        out_specs=pl.BlockSpec((tr, c), lambda i, cref: (cref[0] * nbk + i, 0)))

    def body(cref, q_ref, o_ref):
        acc = q_ref[0].astype(F32) + q_ref[1].astype(F32)
        acc = acc + q_ref[2].astype(F32)
        o_ref[...] = acc + q_ref[3].astype(F32)

    return _pcall(body, name=name, out_shape=_sds((2 * h, c), F32), grid_spec=grid_spec)(cidx, q)


def _join_comm(fulls):
    n = len(fulls)

    def half_copy(outs, sems, a, which):
        x, y, cc = _coords()
        h = outs[a].shape[0] // 2
        rows = outs[a].at[pl.ds((cc if which == 0 else 1 - cc) * h, h)]
        return pltpu.make_async_remote_copy(src_ref=rows, dst_ref=rows, send_sem=sems[0].at[a], recv_sem=sems[1].at[a],
                                            device_id=(x, y, 1 - cc), device_id_type=MESH)

    def start(ins, outs, sems):
        for a in range(n):
            half_copy(outs, sems, a, 0).start()

    def finish(ins, outs, sems):
        for a in range(n):
            half_copy(outs, sems, a, 1).wait_recv()
        for a in range(n):
            half_copy(outs, sems, a, 0).wait_send()

    dma = pltpu.SemaphoreType.DMA
    return _Comm(fulls, [_sds(p.shape, p.dtype) for p in fulls], [dma((n,)), dma((n,))], start, finish,
                 aliases={a: a for a in range(n)})


def _ada_prologue(c, w, b, *, name, comm=None):
    nb, dm = c.shape
    n = w.shape[1]

    def body(c_ref, w_ref, b_ref, call_ref, land_ref, cpad, mod_s, g_send, g_recv, m_send, m_recv):
        x, y, cc = _coords()
        me = 4 * x + 2 * y + cc
        chip = 2 * x + y
        cpad[...] = jnp.zeros_like(cpad)
        cpad[0:nb, :] = c_ref[...]
        copies = []
        for kk in range(1, N_DEV):
            peer = (x ^ (kk >> 2), y ^ ((kk >> 1) & 1), cc ^ (kk & 1))
            cp = pltpu.make_async_remote_copy(src_ref=cpad, dst_ref=call_ref.at[me], send_sem=g_send.at[kk - 1],
                                              recv_sem=g_recv.at[kk - 1], device_id=peer, device_id_type=MESH)
            cp.start()
            copies.append(cp)
        call_ref[me] = cpad[...]
        for kk in range(1, N_DEV):
            pltpu.make_async_remote_copy(src_ref=cpad, dst_ref=call_ref.at[me ^ kk], send_sem=g_send.at[kk - 1],
                                         recv_sem=g_recv.at[kk - 1], device_id=(x, y, cc), device_id_type=MESH).wait_recv()
        cv = call_ref[...].reshape(N_DEV * SUBLANE, dm)
        act = (cv * _sigmoid(cv)).astype(MX)
        mod = jnp.dot(act, w_ref[...].astype(MX), preferred_element_type=F32) + b_ref[...]
        mod_s[...] = mod.reshape(N_DEV, SUBLANE, n)
        for j, (dx, dy) in enumerate(CHIP_RELS):
            cp = pltpu.make_async_remote_copy(src_ref=mod_s.at[4 * (x ^ dx) + 2 * (y ^ dy) + cc], dst_ref=land_ref.at[chip],
                                              send_sem=m_send.at[j], recv_sem=m_recv.at[j],
                                              device_id=(x ^ dx, y ^ dy, cc), device_id_type=MESH)
            cp.start()
            copies.append(cp)
        land_ref[chip] = mod_s[me]
        for j, (dx, dy) in enumerate(CHIP_RELS):
            pltpu.make_async_remote_copy(src_ref=mod_s.at[me], dst_ref=land_ref.at[2 * (x ^ dx) + (y ^ dy)], send_sem=m_send.at[j],
                                         recv_sem=m_recv.at[j], device_id=(x, y, cc), device_id_type=MESH).wait_recv()
        for cp in copies:
            cp.wait_send()

    dma = pltpu.SemaphoreType.DMA
    return _pcall(body, name=name, in_specs=[VMEM_SPEC] * 3, out_specs=[VMEM_SPEC] * 2,
                  out_shape=[_sds((N_DEV, SUBLANE, dm), F32), _sds((N_CHIP, SUBLANE, n), F32)],
                  scratch=[pltpu.VMEM((SUBLANE, dm), F32), pltpu.VMEM((N_DEV, SUBLANE, n), F32),
                           dma((N_DEV - 1,)), dma((N_DEV - 1,)), dma((3,)), dma((3,))], comm=comm)(c, w, b)


def _ada_bwd(c_all, dmod_cols, *, name):
    m, kdim = c_all.shape
    n = dmod_cols.shape[1]
    tn = _pick(n, 1152)

    def body(c_ref, d_ref, o_ref):
        cv = c_ref[...]
        act = (cv * _sigmoid(cv)).astype(MX)
        o_ref[...] = lax.dot_general(act, d_ref[...].astype(MX), (((0,), (0,)), ((), ())), preferred_element_type=F32)

    return _pcall(body, name=name, out_shape=_sds((kdim, n), F32), grid=(n // tn,),
                  in_specs=[pl.BlockSpec((m, kdim), lambda j: (0, 0)), pl.BlockSpec((m, tn), lambda j: (0, j))],
                  out_specs=pl.BlockSpec((kdim, tn), lambda j: (0, j)))(c_all, dmod_cols)


SLAB_W = 1024
RED_ROWS = 8


LOSS_LANE = SLAB_W - LANE


def _pack_stats(st1, st2, st3, st_fin, st_q, st_kv, st_prep, d_pool_scale, loss8, *, name):
    nb = st1.shape[0]
    dm_rows = -(-9 * nb // SUBLANE) * SUBLANE

    def body(s1, s2, s3, sf, sq, skv, sp, sps, loss_ref, o_ref):
        o_ref[...] = jnp.zeros_like(o_ref)
        for s in range(nb):
            rows = [s1[s, 0:1, :], s1[s, 1:2, :], s2[s, 3:4, :], s2[s, 0:1, :], s2[s, 1:2, :], s3[s, 3:4, :], s3[s, 0:1, :],
                    s3[s, 1:2, :], sf[s, 0:1, :]]
            for k, row in enumerate(rows):
                o_ref[9 * s + k:9 * s + k + 1, :] = row

        def over_seq(ref, r):
            acc = ref[0, r:r + 1, :]
            for s in range(1, nb):
                acc = acc + ref[s, r:r + 1, :]
            return acc

        o_ref[dm_rows + 0:dm_rows + 1, :] = over_seq(s1, 2)
        o_ref[dm_rows + 1:dm_rows + 2, :] = over_seq(s2, 2)
        o_ref[dm_rows + 2:dm_rows + 3, :] = over_seq(s3, 2)
        o_ref[dm_rows + 3:dm_rows + 4, 0:POOL_W] = over_seq(sps, 0)
        o_ref[dm_rows + 4:dm_rows + 5, 0:QL] = over_seq(sq, 0)
        o_ref[dm_rows + 5:dm_rows + 6, 0:KVL] = over_seq(skv, 0)
        o_ref[dm_rows + 6:dm_rows + 7, 0:LANE] = over_seq(sp, 0)
        o_ref[dm_rows + 7:dm_rows + 8, 0:LANE] = over_seq(sp, 1)
        o_ref[dm_rows + 7:dm_rows + 8, LOSS_LANE:] = loss_ref[0:1, :]

    return _pcall(body, name=name, out_shape=_sds((dm_rows + RED_ROWS, SLAB_W), F32), in_specs=[VMEM_SPEC] * 9,
                  out_specs=VMEM_SPEC)(st1, st2, st3, st_fin, st_q, st_kv, st_prep, d_pool_scale, loss8)


def _small_allreduce(slab, pool, *, name, comm=None):
    rows, w = slab.shape
    dm = rows - RED_ROWS
    prow, pw = pool.shape
    crow = RED_ROWS + 2 * dm

    def body(slab_ref, pool_ref, red_ref, dmod_ref, ptot_ref, sib_slab, sib_pool, chip_slab, chip_pool, land_slab, land_pool,
             a_send, a_recv, b_send, b_recv):
        x, y, cc = _coords()
        chip = 2 * x + y
        sib = (x, y, 1 - cc)
        to_sib = [pltpu.make_async_remote_copy(src_ref=slab_ref, dst_ref=sib_slab, send_sem=a_send.at[0], recv_sem=a_recv.at[0],
                                               device_id=sib, device_id_type=MESH),
                  pltpu.make_async_remote_copy(src_ref=pool_ref, dst_ref=sib_pool, send_sem=a_send.at[1], recv_sem=a_recv.at[1],
                                               device_id=sib, device_id_type=MESH)]
        for cp in to_sib:
            cp.start()
        for cp in to_sib:
            cp.wait()
        mine_dm, theirs_dm = slab_ref[0:dm, :], sib_slab[0:dm, :]
        chip_slab[0:RED_ROWS, :] = slab_ref[dm:, :] + sib_slab[dm:, :]
        chip_slab[RED_ROWS:RED_ROWS + dm, :] = jnp.where(cc == 0, mine_dm, theirs_dm)
        chip_slab[RED_ROWS + dm:, :] = jnp.where(cc == 0, theirs_dm, mine_dm)
        chip_pool[...] = pool_ref[...] + sib_pool[...]

        sends = []
        for j, (dx, dy) in enumerate(CHIP_RELS):
            peer = (x ^ dx, y ^ dy, cc)
            sends.append(pltpu.make_async_remote_copy(src_ref=chip_slab, dst_ref=land_slab.at[chip], send_sem=b_send.at[j, 0],
                                                      recv_sem=b_recv.at[j, 0], device_id=peer, device_id_type=MESH))
            sends.append(pltpu.make_async_remote_copy(src_ref=chip_pool, dst_ref=land_pool.at[chip], send_sem=b_send.at[j, 1],
                                                      recv_sem=b_recv.at[j, 1], device_id=peer, device_id_type=MESH))
        for cp in sends:
            cp.start()
        land_slab[chip] = chip_slab[...]
        land_pool[chip] = chip_pool[...]
        for j, (dx, dy) in enumerate(CHIP_RELS):
            peer_chip = 2 * (x ^ dx) + (y ^ dy)
            pltpu.make_async_remote_copy(src_ref=chip_slab, dst_ref=land_slab.at[peer_chip], send_sem=b_send.at[j, 0],
                                         recv_sem=b_recv.at[j, 0], device_id=(x, y, cc), device_id_type=MESH).wait_recv()
            pltpu.make_async_remote_copy(src_ref=chip_pool, dst_ref=land_pool.at[peer_chip], send_sem=b_send.at[j, 1],
                                         recv_sem=b_recv.at[j, 1], device_id=(x, y, cc), device_id_type=MESH).wait_recv()
        red = land_slab[0, 0:RED_ROWS, :]
        ptot = land_pool[0]
        for ch in range(1, N_CHIP):
            red = red + land_slab[ch, 0:RED_ROWS, :]
            ptot = ptot + land_pool[ch]
        red_ref[...] = red
        ptot_ref[...] = ptot
        for ch in range(N_CHIP):
            dmod_ref[2 * dm * ch:2 * dm * (ch + 1), :] = land_slab[ch, RED_ROWS:, :]
        for cp in sends:
            cp.wait_send()

    dma = pltpu.SemaphoreType.DMA
    return _pcall(body, name=name, in_specs=[VMEM_SPEC, VMEM_SPEC], out_specs=[VMEM_SPEC] * 3,
                  out_shape=[_sds((RED_ROWS, w), F32), _sds((N_DEV * dm, w), F32), _sds((prow, pw), F32)],
                  scratch=[pltpu.VMEM((rows, w), F32), pltpu.VMEM((prow, pw), F32), pltpu.VMEM((crow, w), F32),
                           pltpu.VMEM((prow, pw), F32), pltpu.VMEM((N_CHIP, crow, w), F32), pltpu.VMEM((N_CHIP, prow, pw), F32),
                           dma((2,)), dma((2,)), dma((3, 2)), dma((3, 2))], comm=comm)(slab, pool)


def _adamw_math(w, g, m, v):
    mn = ADAM_B1 * m + (1.0 - ADAM_B1) * g
    vn = ADAM_B2 * v + (1.0 - ADAM_B2) * (g * g)
    bc1 = 1.0 / (1.0 - ADAM_B1 ** ADAM_STEP)
    bc2 = 1.0 / (1.0 - ADAM_B2 ** ADAM_STEP)
    return -ADAM_LR * ((mn * bc1) / (jnp.sqrt(vn * bc2) + ADAM_EPS) + ADAM_WD * w), mn, vn


def _small_update(red, dmod_all, pool_total, nb, params, *, name):
    names = list(SMALL)
    dm = dmod_all.shape[0] // N_DEV

    def grad_of(nm, red_ref, dmod_ref, ptot_ref):
        if nm == "b_ada":
            acc = None
            for d in range(N_DEV):
                for s in range(nb):
                    blk = dmod_ref[d * dm + 9 * s:d * dm + 9 * s + 9, :]
                    acc = blk if acc is None else acc + blk
            return jnp.concatenate([acc[k:k + 1, :] for k in range(9)], axis=1)
        if nm == "pool_grp":
            return ptot_ref[...]
        row, lo, n = {"norm_ffn1": (0, 0, D), "norm_mix": (1, 0, D), "norm_ffn2": (2, 0, D), "pool_scale": (3, 0, POOL_W),
                      "q_a_norm": (4, 0, QL), "kv_a_norm": (5, 0, KVL), "q_norm_nope": (6, 0, NOPE),
                      "q_norm_rope": (6, NOPE, ROPE), "k_norm_nope": (7, 0, NOPE), "k_norm_rope": (7, KR_LANE, ROPE)}[nm]
        return red_ref[row:row + 1, lo:lo + n]

    def body(*refs):
        red_ref, dmod_ref, ptot_ref = refs[:3]
        ins = refs[3:3 + 3 * len(names)]
        outs = refs[3 + 3 * len(names):]
        outs[4 * len(names)][...] = red_ref[RED_ROWS - 1:RED_ROWS, LOSS_LANE:]
        for i, nm in enumerate(names):
            g = grad_of(nm, red_ref, dmod_ref, ptot_ref)
            d, mn, vn = _adamw_math(ins[3 * i][...], g, ins[3 * i + 1][...], ins[3 * i + 2][...])
            outs[4 * i][...] = g
            outs[4 * i + 1][...] = d
            outs[4 * i + 2][...] = mn
            outs[4 * i + 3][...] = vn

    flat_in = [a for nm in names for a in params[nm]]
    out_shape = [_sds(params[nm][0].shape, F32) for nm in names for _ in range(4)] + [_sds((1, LANE), F32)]
    res = _pcall(body, name=name, in_specs=[VMEM_SPEC] * (3 + len(flat_in)), out_specs=[VMEM_SPEC] * len(out_shape),
                 out_shape=out_shape)(red, dmod_all, pool_total, *flat_in)
    return {nm: tuple(res[4 * i:4 * i + 4]) for i, nm in enumerate(names)}, res[-1]


def _adamw(w, g, m, v, *, name, comm=None):
    r, c = w.shape
    tr = _pick_rows(r, 256)
    tc = c if tr < r else _pick(c, 256)
    spec = pl.BlockSpec((tr, tc), lambda i, j: (i, j))
    bc1 = 1.0 / (1.0 - ADAM_B1 ** ADAM_STEP)
    bc2 = 1.0 / (1.0 - ADAM_B2 ** ADAM_STEP)

    def body(w_ref, g_ref, m_ref, v_ref, d_ref, mo_ref, vo_ref):
        gv = g_ref[...]
        mn = ADAM_B1 * m_ref[...] + (1.0 - ADAM_B1) * gv
        vn = ADAM_B2 * v_ref[...] + (1.0 - ADAM_B2) * (gv * gv)
        mo_ref[...] = mn
        vo_ref[...] = vn
        d_ref[...] = -ADAM_LR * ((mn * bc1) / (jnp.sqrt(vn * bc2) + ADAM_EPS) + ADAM_WD * w_ref[...])

    out = _sds((r, c), F32)
    return _pcall(body, name=name, out_shape=[out, out, out], grid=(r // tr, c // tc), in_specs=[spec] * 4, out_specs=[spec] * 3,
                  comm=comm)(w, g, m, v)


BIG = ("w_ffn1_in", "w_ffn1_out", "w_in", "w_pool_proj", "w_q_up", "w_kv_up", "w_mla_proj", "w_out", "w_ffn2_in", "w_ffn2_out")
ROW_SHARDED = ("w_ffn1_out", "w_out", "w_ffn2_out")
KERNEL_NAME = {"w_ffn1_in": "ffn1_in", "w_ffn1_out": "ffn1_out", "w_in": "w_in", "w_pool_proj": "pool_proj", "w_q_up": "q_up",
               "w_kv_up": "kv_up", "w_mla_proj": "mla_proj", "w_out": "w_out", "w_ffn2_in": "ffn2_in", "w_ffn2_out": "ffn2_out"}
WEIGHTS = ("w_ada", "b_ada", "norm_ffn1", "w_ffn1_in", "w_ffn1_out", "norm_mix", "w_in", "pool_grp", "pool_scale", "w_pool_proj",
           "q_a_norm", "w_q_up", "kv_a_norm", "w_kv_up", "q_norm_nope", "q_norm_rope", "k_norm_nope", "k_norm_rope", "w_mla_proj",
           "w_out", "norm_ffn2", "w_ffn2_in", "w_ffn2_out")
SMALL = ("b_ada", "norm_ffn1", "norm_mix", "pool_grp", "pool_scale", "q_a_norm", "kv_a_norm", "q_norm_nope", "q_norm_rope",
         "k_norm_nope", "k_norm_rope", "norm_ffn2")
SMALL_SEQ = tuple(n for n in SMALL if n != "pool_grp")
SLAB_W = 1024


def _assemble(name, stacked):
    if name in ROW_SHARDED:
        return stacked.reshape(stacked.shape[0] * stacked.shape[1], stacked.shape[2])
    return jnp.transpose(stacked, (1, 0, 2)).reshape(stacked.shape[1], stacked.shape[0] * stacked.shape[2])


def _split(name, full):
    if name in ROW_SHARDED:
        return full.reshape(N_CHIP, full.shape[0] // N_CHIP, full.shape[1])
    return jnp.transpose(full.reshape(full.shape[0], N_CHIP, full.shape[1] // N_CHIP), (1, 0, 2))


GU = ("w_ffn1_in", "w_ffn2_in")
NARROW = ("w_in", "w_q_up")


def _to_rows(v, width=SLAB_W):
    flat = v.reshape(-1)
    rows = -(-flat.shape[0] // width)
    return jnp.pad(flat, (0, rows * width - flat.shape[0])).reshape(rows, width)


def _pack_small(parts, names=SMALL):
    rows, spans, at = [], {}, 0
    for name in names:
        r = _to_rows(parts[name])
        spans[name] = (at, parts[name].size, parts[name].shape)
        rows.append(r)
        at += r.shape[0]
    pad = (-at) % SUBLANE
    if pad:
        rows.append(jnp.zeros((pad, SLAB_W), F32))
    return jnp.concatenate(rows, axis=0), spans


def _unpack_small(slab, spans):
    out = {}
    for name, (at, size, shape) in spans.items():
        nrow = -(-size // SLAB_W)
        out[name] = slab[at:at + nrow].reshape(-1)[:size].reshape(shape)
    return out


GATHER_ON = {"ada_prologue": ("w_ffn1_in",),
             "ffn1_up": ("w_ffn1_out", "w_in"),
             "mix_in": ("w_pool_proj", "w_q_up", "w_kv_up", "w_mla_proj", "w_out"),
             "attn_fwd": ("w_ffn2_in", "w_ffn2_out")}
SWAP_ON = {"bwd_norm3": ("w_ffn2_out", "w_ffn2_in"),
           "d_mix_h": ("w_out", "w_pool_proj", "w_mla_proj", "w_q_up", "w_kv_up", "w_in"),
           "ffn1_dact": ("w_ffn1_out",),
           "d_ffn1_h": ("w_ffn1_in",)}
EXCHANGE_ON = {"attn_bwd": ("bwd_norm3", ("w_ffn2_out", "w_ffn2_in")),
               "bwd_norm2": ("d_mix_h", ("w_out", "w_pool_proj", "w_mla_proj", "w_q_up", "w_kv_up")),
               "d_ffn1_out": ("d_mix_h", ("w_in",)),
               "d_ffn1_in": ("ffn1_dact", ("w_ffn1_out",)),
               "bwd_norm1": ("d_ffn1_h", ("w_ffn1_in",))}


class _ExchangePlan:
    def __init__(self, shards, cidx):
        self.shards, self.cidx = shards, cidx
        self.W, self.G, self.parts, self.pre, self.reduced = {}, {}, {}, {}, {}

    def grad(self, key, g):
        self.G[key] = g

    def rider(self, name):
        if name in GATHER_ON:
            return _gather_comm([self.shards[n] for n in GATHER_ON[name]])
        if name in SWAP_ON:
            self.parts[name] = [self._stacked(n) for n in SWAP_ON[name]]
            return _swap_comm(self.parts[name])
        if name in EXCHANGE_ON:
            swap_host, names = EXCHANGE_ON[name]
            return _exchange_comm([self.pre[swap_host][SWAP_ON[swap_host].index(n)] for n in names])
        return None

    def landed(self, name, outs):
        if name in GATHER_ON:
            for n, g in zip(GATHER_ON[name], outs):
                self.W[KERNEL_NAME[n]] = self._to_kernel(n, g)
        elif name in SWAP_ON:
            self.pre[name] = [_add_half(p, o, self.cidx, name="rs_add_" + KERNEL_NAME[n])
                              for n, p, o in zip(SWAP_ON[name], self.parts[name], outs)]
        elif name in EXCHANGE_ON:
            for n, q in zip(EXCHANGE_ON[name][1], outs):
                self.reduced[n] = _sum_chips(q, self.cidx, name="rs_sum_" + KERNEL_NAME[n])

    @staticmethod
    def _to_kernel(n, stacked):
        if n in GU:
            return stacked
        full = _assemble(n, stacked)
        return {"w_in": _w_in_to_kernel, "w_q_up": _q_up_to_kernel, "w_kv_up": _kv_up_to_kernel}.get(n, lambda w: w)(full)

    def _stacked(self, n):
        g = self.G[KERNEL_NAME[n]]
        if n in GU:
            return g
        full = {"w_in": _w_in_from_kernel, "w_q_up": _q_up_from_kernel, "w_kv_up": _kv_up_from_kernel}.get(n, lambda w: w)(g)
        return _split(n, full)


def kernel(x, c, positions, w_ada, b_ada, norm_ffn1, w_ffn1_in, w_ffn1_out, norm_mix, w_in, pool_grp, pool_scale, w_pool_proj, q_a_norm, w_q_up, kv_a_norm, w_kv_up, q_norm_nope, q_norm_rope, k_norm_nope, k_norm_rope, w_mla_proj, w_out, norm_ffn2, w_ffn2_in, w_ffn2_out, loss_target, m_w_ada, m_b_ada, m_norm_ffn1, m_w_ffn1_in, m_w_ffn1_out, m_norm_mix, m_w_in, m_pool_grp, m_pool_scale, m_w_pool_proj, m_q_a_norm, m_w_q_up, m_kv_a_norm, m_w_kv_up, m_q_norm_nope, m_q_norm_rope, m_k_norm_nope, m_k_norm_rope, m_w_mla_proj, m_w_out, m_norm_ffn2, m_w_ffn2_in, m_w_ffn2_out, v_w_ada, v_b_ada, v_norm_ffn1, v_w_ffn1_in, v_w_ffn1_out, v_norm_mix, v_w_in, v_pool_grp, v_pool_scale, v_w_pool_proj, v_q_a_norm, v_w_q_up, v_kv_a_norm, v_w_kv_up, v_q_norm_nope, v_q_norm_rope, v_k_norm_nope, v_k_norm_rope, v_w_mla_proj, v_w_out, v_norm_ffn2, v_w_ffn2_in, v_w_ffn2_out):
    args = dict(locals())
    wts = {n: args[n][0] for n in WEIGHTS}
    mom = {n: args["m_" + n][0] for n in WEIGHTS}
    var = {n: args["v_" + n][0] for n in WEIGHTS}
    nb, seq, dm = x.shape
    tokens = nb * seq
    xi, yi, ci = _coords()
    chip = 2 * xi + yi
    dev = 4 * xi + 2 * yi + ci

    cidx = ci.astype(jnp.int32).reshape(1)
    plan = _ExchangePlan({n: wts[n].astype(MX) for n in BIG}, cidx)
    plan.W["pool_grp"] = wts["pool_grp"].astype(MX)

    ncol = w_ada.shape[2]
    b_cols = lax.dynamic_slice_in_dim(wts["b_ada"].reshape(1, -1), chip * ncol, ncol, axis=1)
    c_slots, mod_slots = _run(plan, _ada_prologue, c, wts["w_ada"], b_cols, name="ada_prologue")
    c_all = c_slots[:, :nb].reshape(N_DEV * nb, dm)
    mod3 = jnp.transpose(mod_slots[:, :nb], (1, 0, 2)).reshape(nb, 9, dm)
    P = {"norm_ffn1": wts["norm_ffn1"].reshape(1, dm), "norm_mix": wts["norm_mix"].reshape(1, dm),
         "norm_ffn2": wts["norm_ffn2"].reshape(1, dm), "pool_scale": wts["pool_scale"].reshape(1, POOL_W),
         "q_a_norm": wts["q_a_norm"].reshape(1, QL), "kv_a_norm": wts["kv_a_norm"].reshape(1, KVL),
         "q_gain": _gain_slab(wts["q_norm_nope"].reshape(1, NOPE), wts["q_norm_rope"].reshape(1, ROPE)),
         "k_gain": _gain_slab(wts["k_norm_nope"].reshape(1, NOPE), wts["k_norm_rope"].reshape(1, ROPE))}
    cos, sin = _rope_tables(positions.reshape(tokens))

    loss8, grad_x, stats, d_pool_grp = _layer_fwd_bwd(x.reshape(tokens, dm), loss_target.reshape(tokens, dm), mod3, cos, sin, plan, P)

    slab = _pack_stats(*stats, loss8, name="pack_stats")
    red, dmod_rows, pool_total = _run(plan, _small_allreduce, slab, d_pool_grp.reshape(POOL_G * LANE, LANE), name="small_allreduce")
    joined = _comm_only(_join_comm([plan.reduced[n] for n in BIG]), name="rs_join_halves")
    grads = {n: g for n, g in zip(BIG, joined)}
    dm_rows = dmod_rows.shape[0] // N_DEV
    dmod_all = dmod_rows.reshape(N_DEV, dm_rows, SLAB_W)[:, :9 * nb].reshape(N_DEV * nb, 9 * dm)
    dmod_cols = lax.dynamic_slice_in_dim(dmod_all, chip * ncol, ncol, axis=1)
    grads["w_ada"] = _ada_bwd(c_all, dmod_cols, name="ada_bwd")

    delta, new_m, new_v = {}, {}, {}
    as2d = lambda a: a.reshape(POOL_G * LANE, LANE) if a.ndim == 4 else a.reshape(1, -1)
    upd, loss_row = _small_update(red, dmod_rows, pool_total, nb,
                                  {n: tuple(as2d(args[p + n]) for p in ("", "m_", "v_")) for n in SMALL}, name="small_update")
    loss = loss_row[0, 0]
    for n in SMALL:
        grads[n], delta[n], new_m[n], new_v[n] = upd[n]
    for n in ("w_ada",) + BIG:
        if n in NARROW:
            res = _adamw(wts[n].T, grads[n].T, mom[n].T, var[n].T, name="adamw_" + n)
            delta[n], new_m[n], new_v[n] = (r.T for r in res)
        else:
            delta[n], new_m[n], new_v[n] = _adamw(wts[n], grads[n], mom[n], var[n], name="adamw_" + n)

    def lead(a, n):
        return a.reshape((1,) + wts[n].shape)

    return (loss, grad_x.reshape(nb, seq, dm), *[lead(grads[n], n) for n in WEIGHTS], *[lead(delta[n], n) for n in WEIGHTS],
            *[lead(new_m[n], n) for n in WEIGHTS], *[lead(new_v[n], n) for n in WEIGHTS])
```

```python
import functools
import math

import jax
import jax.numpy as jnp
from jax import lax
from jax.experimental import pallas as pl
from jax.experimental.pallas import tpu as pltpu

F32 = jnp.float32
MX = jnp.bfloat16

D = 1024
DFF = 2816
NH = 8
POOL_W = 512
POOL_G = 4
QL = 384
KVL = 256
ROPE = 32
NOPE = 64
LANE = 128
SUBLANE = 8
EPS = 1e-6
ATTN_SCALE = 1.0 / math.sqrt(96.0)
NEG = -1e30

Z_UP, Z_QL, Z_KV, Z_KR, Z_GP, Z_GM, Z_W = 0, 512, 896, 1152, 1280, 2304, 3328
KR_LANE = 64
LAT_W = 1280

ADAM_LR, ADAM_B1, ADAM_B2, ADAM_EPS, ADAM_WD, ADAM_STEP = 0.001, 0.9, 0.999, 1e-08, 0.01, 10

VMEM_LIMIT = 48 * 1024 * 1024
MESH = pl.DeviceIdType.MESH
N_DEV = 8
N_CHIP = 4


class _Comm:
    def __init__(self, ins, out_shapes, sems, start, finish, aliases=None):
        self.ins, self.out_shapes, self.sems = list(ins), list(out_shapes), list(sems)
        self.start, self.finish = start, finish
        self.aliases = aliases or {}


def _pcall(body, *, name, out_shape, grid=(), in_specs=None, out_specs=None, scratch=(), grid_spec=None, aliases=None,
           comm=None):
    params = pltpu.CompilerParams(vmem_limit_bytes=VMEM_LIMIT)
    kw = dict(name=name, compiler_params=params)
    if comm is None:
        if aliases:
            kw["input_output_aliases"] = aliases
        if grid_spec is not None:
            return pl.pallas_call(body, grid_spec=grid_spec, out_shape=out_shape, **kw)
        return pl.pallas_call(body, grid=grid, in_specs=in_specs, out_specs=out_specs, scratch_shapes=scratch,
                              out_shape=out_shape, **kw)
    single = not isinstance(out_shape, (list, tuple))
    outs = [out_shape] if single else list(out_shape)
    ospecs = [out_specs] if single else list(out_specs)
    n_in, n_out, n_ci, n_co, n_scr = len(in_specs), len(outs), len(comm.ins), len(comm.out_shapes), len(scratch)
    io = dict(aliases or {})
    io.update({n_in + i: n_out + o for i, o in comm.aliases.items()})

    def riding(*refs):
        ins, cins = refs[:n_in], refs[n_in:n_in + n_ci]
        at = n_in + n_ci
        os_, couts = refs[at:at + n_out], refs[at + n_out:at + n_out + n_co]
        at += n_out + n_co
        scr, csems = refs[at:at + n_scr], refs[at + n_scr:]
        if grid:
            first = functools.reduce(jnp.logical_and, [pl.program_id(d) == 0 for d in range(len(grid))])
            last = functools.reduce(jnp.logical_and, [pl.program_id(d) == grid[d] - 1 for d in range(len(grid))])
            pl.when(first)(lambda: comm.start(cins, couts, csems))
            body(*ins, *os_, *scr)
            pl.when(last)(lambda: comm.finish(cins, couts, csems))
        else:
            comm.start(cins, couts, csems)
            body(*ins, *os_, *scr)
            comm.finish(cins, couts, csems)

    call = pl.pallas_call(riding, grid=grid, in_specs=list(in_specs) + [HBM_SPEC] * n_ci, out_specs=ospecs + [HBM_SPEC] * n_co,
                          out_shape=outs + comm.out_shapes, scratch_shapes=list(scratch) + comm.sems,
                          input_output_aliases=io, **kw)

    def run(*args):
        res = call(*args, *comm.ins)
        main = list(res[:n_out])
        return (main[0] if single else main), list(res[n_out:])

    return run


def _pick(dim, target):
    best = None
    for t in range(LANE, min(dim, target) + 1, LANE):
        if dim % t == 0:
            best = t
    return dim if best is None else best


def _sds(shape, dtype):
    return jax.ShapeDtypeStruct(shape, dtype)


def _dw(a, g, *, name, tn=1024, comm=None):
    return _mm(a, g, mode="tn", out_dtype=F32, name=name, tm=256, tn=tn, tk=a.shape[0], n_outer=True, comm=comm)


def _mm(a, b, *, mode, out_dtype, name, tm=1024, tn=1024, tk=4096, n_outer=False, comm=None):
    if mode == "nn":
        (M, K), (K2, N) = a.shape, b.shape
    elif mode == "nt":
        (M, K), (N, K2) = a.shape, b.shape
    else:
        (K, M), (K2, N) = a.shape, b.shape
    assert K == K2, (name, a.shape, b.shape)
    tm, tn, tk = _pick(M, tm), _pick(N, tn), _pick(K, tk)
    nk = K // tk
    if n_outer:
        ij = lambda g0, g1: (g1, g0)
        grid = (N // tn, M // tm, nk)
    else:
        ij = lambda g0, g1: (g0, g1)
        grid = (M // tm, N // tn, nk)
    if mode == "tn":
        a_spec = pl.BlockSpec((tk, tm), lambda g0, g1, k: (k, ij(g0, g1)[0]))
    else:
        a_spec = pl.BlockSpec((tm, tk), lambda g0, g1, k: (ij(g0, g1)[0], k))
    if mode == "nt":
        b_spec = pl.BlockSpec((tn, tk), lambda g0, g1, k: (ij(g0, g1)[1], k))
    else:
        b_spec = pl.BlockSpec((tk, tn), lambda g0, g1, k: (k, ij(g0, g1)[1]))
    o_spec = pl.BlockSpec((tm, tn), lambda g0, g1, k: ij(g0, g1))
    dn = {"nn": (((1,), (0,)), ((), ())), "nt": (((1,), (1,)), ((), ())), "tn": (((0,), (0,)), ((), ()))}[mode]

    def dot(a_ref, b_ref):
        return lax.dot_general(a_ref[...].astype(MX), b_ref[...].astype(MX), dn, preferred_element_type=F32)

    def body_one(a_ref, b_ref, o_ref):
        o_ref[...] = dot(a_ref, b_ref).astype(o_ref.dtype)

    def body_acc(a_ref, b_ref, o_ref, acc_ref):
        k = pl.program_id(2)
        part = dot(a_ref, b_ref)

        @pl.when(k == 0)
        def _():
            acc_ref[...] = part

        @pl.when(k > 0)
        def _():
            acc_ref[...] += part

        @pl.when(k == nk - 1)
        def _():
            o_ref[...] = acc_ref[...].astype(o_ref.dtype)

    return _pcall(body_one if nk == 1 else body_acc, name=name, out_shape=_sds((M, N), out_dtype), grid=grid,
                  in_specs=[a_spec, b_spec], out_specs=o_spec, scratch=[] if nk == 1 else [pltpu.VMEM((tm, tn), F32)],
                  comm=comm)(a, b)


def _gu_shard(q):
    return (q % 2) * 2 + q // 2


def _ffn_up(h, w_st, *, name, tm=512, comm=None):
    T, dm = h.shape
    hw = w_st.shape[2]
    tm = _pick(T, tm)

    def body(h_ref, wg_ref, wu_ref, gu_ref, a_ref):
        hv = h_ref[...]
        g = jnp.dot(hv, wg_ref[0], preferred_element_type=F32)
        u = jnp.dot(hv, wu_ref[0], preferred_element_type=F32)
        gu_ref[:, :hw] = g.astype(gu_ref.dtype)
        gu_ref[:, hw:] = u.astype(gu_ref.dtype)
        a_ref[...] = (g * _sigmoid(g) * u).astype(a_ref.dtype)

    return _pcall(body, name=name, grid=(T // tm, 2),
                  in_specs=[pl.BlockSpec((tm, dm), lambda i, j: (i, 0)), pl.BlockSpec((1, dm, hw), lambda i, j: (j, 0, 0)),
                            pl.BlockSpec((1, dm, hw), lambda i, j: (2 + j, 0, 0))],
                  out_specs=[pl.BlockSpec((tm, 2 * hw), lambda i, j: (i, j)), pl.BlockSpec((tm, hw), lambda i, j: (i, j))],
                  out_shape=[_sds((T, 4 * hw), MX), _sds((T, 2 * hw), MX)], comm=comm)(h, w_st, w_st)


def _ffn_up_normed(x, mod3, gain, w_st, *, sub, name, tm=512, comm=None):
    T, dm = x.shape
    hw = w_st.shape[2]
    tm = _pick(T, tm)
    tps = (T // mod3.shape[0]) // tm

    def body(x_ref, mod_ref, n_ref, wg_ref, wu_ref, h_ref, gu_ref, a_ref):
        @pl.when(pl.program_id(1) == 0)
        def _():
            xv = x_ref[...]
            xn = xv * _rsq(xv) * n_ref[...]
            h_ref[...] = (xn * (1.0 + mod_ref[0, 3 * sub + 1:3 * sub + 2, :]) + mod_ref[0, 3 * sub:3 * sub + 1, :]).astype(h_ref.dtype)

        hv = h_ref[...]
        g = jnp.dot(hv, wg_ref[0], preferred_element_type=F32)
        u = jnp.dot(hv, wu_ref[0], preferred_element_type=F32)
        gu_ref[:, :hw] = g.astype(gu_ref.dtype)
        gu_ref[:, hw:] = u.astype(gu_ref.dtype)
        a_ref[...] = (g * _sigmoid(g) * u).astype(a_ref.dtype)

    return _pcall(body, name=name, grid=(T // tm, 2),
                  in_specs=[pl.BlockSpec((tm, dm), lambda i, j: (i, 0)), pl.BlockSpec((1, 9, dm), lambda i, j: (i // tps, 0, 0)),
                            pl.BlockSpec((1, dm), lambda i, j: (0, 0)), pl.BlockSpec((1, dm, hw), lambda i, j: (j, 0, 0)),
                            pl.BlockSpec((1, dm, hw), lambda i, j: (2 + j, 0, 0))],
                  out_specs=[pl.BlockSpec((tm, dm), lambda i, j: (i, 0)), pl.BlockSpec((tm, 2 * hw), lambda i, j: (i, j)),
                             pl.BlockSpec((tm, hw), lambda i, j: (i, j))],
                  out_shape=[_sds((T, dm), MX), _sds((T, 4 * hw), MX), _sds((T, 2 * hw), MX)],
                  comm=comm)(x, mod3, gain, w_st, w_st)


def _ffn_dact(df, gu, w_out, *, name, tm=512, comm=None):
    T, dm = df.shape
    hw = gu.shape[1] // 4
    tm = _pick(T, tm)

    def body(df_ref, gu_ref, wo_ref, dgu_ref):
        da = lax.dot_general(df_ref[...], wo_ref[...], (((1,), (1,)), ((), ())), preferred_element_type=F32)
        g = gu_ref[:, :hw].astype(F32)
        u = gu_ref[:, hw:].astype(F32)
        s = _sigmoid(g)
        dgu_ref[:, :hw] = (da * u * (s * (1.0 + g * (1.0 - s)))).astype(dgu_ref.dtype)
        dgu_ref[:, hw:] = (da * (g * s)).astype(dgu_ref.dtype)

    return _pcall(body, name=name, grid=(T // tm, 2),
                  in_specs=[pl.BlockSpec((tm, dm), lambda i, j: (i, 0)), pl.BlockSpec((tm, 2 * hw), lambda i, j: (i, j)),
                            pl.BlockSpec((hw, dm), lambda i, j: (j, 0))],
                  out_specs=pl.BlockSpec((tm, 2 * hw), lambda i, j: (i, j)), out_shape=_sds(gu.shape, MX),
                  comm=comm)(df, gu, w_out)


def _ffn_dh(dgu, w_st, *, name, tm=1024, comm=None):
    T = dgu.shape[0]
    _, dm, hw = w_st.shape
    tm = _pick(T, tm)

    def body(d_ref, w_ref, o_ref, acc_ref):
        q = pl.program_id(1)
        part = lax.dot_general(d_ref[...], w_ref[0], (((1,), (1,)), ((), ())), preferred_element_type=F32)

        @pl.when(q == 0)
        def _():
            acc_ref[...] = part

        @pl.when(q > 0)
        def _():
            acc_ref[...] += part

        @pl.when(q == 3)
        def _():
            o_ref[...] = acc_ref[...]

    return _pcall(body, name=name, grid=(T // tm, 4),
                  in_specs=[pl.BlockSpec((tm, hw), lambda i, q: (i, q)), pl.BlockSpec((1, dm, hw), lambda i, q: (_gu_shard(q), 0, 0))],
                  out_specs=pl.BlockSpec((tm, dm), lambda i, q: (i, 0)), out_shape=_sds((T, dm), F32),
                  scratch=[pltpu.VMEM((tm, dm), F32)], comm=comm)(dgu, w_st)


def _ffn_dw_in(h, dgu, *, name, tm=256, comm=None):
    T, dm = h.shape
    hw = dgu.shape[1] // 4
    tm = _pick(dm, tm)

    def body(h_ref, d_ref, o_ref):
        o_ref[0] = lax.dot_general(h_ref[...], d_ref[...], (((0,), (0,)), ((), ())), preferred_element_type=F32)

    return _pcall(body, name=name, grid=(4, dm // tm),
                  in_specs=[pl.BlockSpec((T, tm), lambda q, i: (0, i)), pl.BlockSpec((T, hw), lambda q, i: (0, q))],
                  out_specs=pl.BlockSpec((1, tm, hw), lambda q, i: (_gu_shard(q), i, 0)),
                  out_shape=_sds((4, dm, hw), F32), comm=comm)(h, dgu)


def _rsq(x):
    return lax.rsqrt(jnp.mean(x * x, axis=-1, keepdims=True) + EPS)


def _sigmoid(x):
    return 1.0 / (1.0 + jnp.exp(-x))


def _row_spec(tm, w):
    return pl.BlockSpec((tm, w), lambda i: (i, 0))


def _fwd_block(x_prev, f_prev, mod3, gain, *, sub, coef, name, tm=256):
    T, dm = x_prev.shape
    tps = (T // mod3.shape[0]) // tm
    has_f = f_prev is not None
    mod_spec = pl.BlockSpec((1, 9, dm), lambda i: (i // tps, 0, 0))
    vec_spec = pl.BlockSpec((1, dm), lambda i: (0, 0))

    def body(*refs):
        if has_f:
            x_ref, f_ref, mod_ref, n_ref, xo_ref, h_ref = refs
            x = x_ref[...] + coef * mod_ref[0, 3 * sub - 1:3 * sub, :] * f_ref[...]
            xo_ref[...] = x
        else:
            x_ref, mod_ref, n_ref, h_ref = refs
            x = x_ref[...]
        xn = x * _rsq(x) * n_ref[...]
        h = xn * (1.0 + mod_ref[0, 3 * sub + 1:3 * sub + 2, :]) + mod_ref[0, 3 * sub:3 * sub + 1, :]
        h_ref[...] = h.astype(h_ref.dtype)

    row = _row_spec(tm, dm)
    if has_f:
        return _pcall(body, name=name, grid=(T // tm,), in_specs=[row, row, mod_spec, vec_spec], out_specs=[row, row],
                      out_shape=[_sds((T, dm), F32), _sds((T, dm), MX)])(x_prev, f_prev, mod3, gain)
    return _pcall(body, name=name, grid=(T // tm,), in_specs=[row, mod_spec, vec_spec], out_specs=row,
                  out_shape=_sds((T, dm), MX))(x_prev, mod3, gain)


def _mm_resid_norm(a, w, x_prev, mod3, gain, *, sub, coef, name, tm=512, comm=None):
    T, k = a.shape
    dm = w.shape[1]
    tm = _pick(T, tm)
    tps = (T // mod3.shape[0]) // tm

    def body(a_ref, w_ref, x_ref, mod_ref, n_ref, f_ref, xo_ref, h_ref):
        f = jnp.dot(a_ref[...], w_ref[...], preferred_element_type=F32)
        f_ref[...] = f
        x = x_ref[...] + coef * mod_ref[0, 3 * sub - 1:3 * sub, :] * f
        xo_ref[...] = x
        xn = x * _rsq(x) * n_ref[...]
        h_ref[...] = (xn * (1.0 + mod_ref[0, 3 * sub + 1:3 * sub + 2, :]) + mod_ref[0, 3 * sub:3 * sub + 1, :]).astype(h_ref.dtype)

    row = _row_spec(tm, dm)
    return _pcall(body, name=name, grid=(T // tm,),
                  in_specs=[_row_spec(tm, k), pl.BlockSpec((k, dm), lambda i: (0, 0)), row,
                            pl.BlockSpec((1, 9, dm), lambda i: (i // tps, 0, 0)), pl.BlockSpec((1, dm), lambda i: (0, 0))],
                  out_specs=[row, row, row], out_shape=[_sds((T, dm), F32), _sds((T, dm), F32), _sds((T, dm), MX)],
                  comm=comm)(a, w, x_prev, mod3, gain)


def _mm_loss(a, w, x2, tgt, mod3, *, name, tm=512):
    T, k = a.shape
    dm = w.shape[1]
    tm = _pick(T, tm)
    tps = (T // mod3.shape[0]) // tm
    mod_spec = pl.BlockSpec((1, 9, dm), lambda i: (i // tps, 0, 0))
    row = _row_spec(tm, dm)
    stat_spec = pl.BlockSpec((1, SUBLANE, dm), lambda i: (i // tps, 0, 0))
    loss_spec = pl.BlockSpec((SUBLANE, LANE), lambda i: (0, 0))

    def body(a_ref, w_ref, x_ref, t_ref, mod_ref, dy_ref, df_ref, st_ref, loss_ref):
        i = pl.program_id(0)
        g = mod_ref[0, 8:9, :]
        f = jnp.dot(a_ref[...], w_ref[...], preferred_element_type=F32)
        err = x_ref[...] + 0.5 * g * f - t_ref[...]
        dy = err * (1.0 / dm)
        dy_ref[...] = dy
        df_ref[...] = (0.5 * g * dy).astype(df_ref.dtype)
        dgate = jnp.sum(0.5 * dy * f, axis=0, keepdims=True)
        part = 0.5 * jnp.sum(jnp.sum(err * err, axis=0, keepdims=True), axis=1, keepdims=True) * (1.0 / dm)

        @pl.when(i % tps == 0)
        def _():
            st_ref[...] = jnp.zeros_like(st_ref)

        @pl.when(i == 0)
        def _():
            loss_ref[...] = jnp.zeros_like(loss_ref)

        st_ref[0, 0:1, :] += dgate
        loss_ref[...] += jnp.broadcast_to(part, loss_ref.shape)

    return _pcall(body, name=name, grid=(T // tm,),
                  in_specs=[_row_spec(tm, k), pl.BlockSpec((k, dm), lambda i: (0, 0)), row, row, mod_spec],
                  out_specs=[row, row, stat_spec, loss_spec],
                  out_shape=[_sds((T, dm), F32), _sds((T, dm), MX), _sds((mod3.shape[0], SUBLANE, dm), F32),
                             _sds((SUBLANE, LANE), F32)])(a, w, x2, tgt, mod3)


def _bwd_block(x_cur, dh, dx_in, f_prev, mod3, gain, *, sub, coef, name, tm=256, comm=None):
    T, dm = x_cur.shape
    nb = mod3.shape[0]
    tps = (T // nb) // tm
    has_f = f_prev is not None
    mod_spec = pl.BlockSpec((1, 9, dm), lambda i: (i // tps, 0, 0))
    vec_spec = pl.BlockSpec((1, dm), lambda i: (0, 0))
    stat_spec = pl.BlockSpec((1, SUBLANE, dm), lambda i: (i // tps, 0, 0))
    row = _row_spec(tm, dm)

    def body(*refs):
        if has_f:
            x_ref, dh_ref, dxi_ref, f_ref, mod_ref, n_ref, dx_ref, df_ref, st_ref = refs
        else:
            x_ref, dh_ref, dxi_ref, mod_ref, n_ref, dx_ref, st_ref = refs
        i = pl.program_id(0)
        x = x_ref[...]
        r = _rsq(x)
        xhat = x * r
        n = n_ref[...]
        dhv = dh_ref[...]
        d_shift = jnp.sum(dhv, axis=0, keepdims=True)
        d_scale = jnp.sum(dhv * (xhat * n), axis=0, keepdims=True)
        dxn = dhv * (1.0 + mod_ref[0, 3 * sub + 1:3 * sub + 2, :])
        d_gain = jnp.sum(dxn * xhat, axis=0, keepdims=True)
        dxhat = dxn * n
        dx = dxi_ref[...] + r * (dxhat - xhat * jnp.mean(dxhat * xhat, axis=-1, keepdims=True))
        dx_ref[...] = dx

        @pl.when(i % tps == 0)
        def _():
            st_ref[...] = jnp.zeros_like(st_ref)

        st_ref[0, 0:1, :] += d_shift
        st_ref[0, 1:2, :] += d_scale
        st_ref[0, 2:3, :] += d_gain
        if has_f:
            f = f_ref[...]
            st_ref[0, 3:4, :] += jnp.sum(coef * dx * f, axis=0, keepdims=True)
            df_ref[...] = (coef * mod_ref[0, 3 * sub - 1:3 * sub, :] * dx).astype(df_ref.dtype)

    st_shape = _sds((nb, SUBLANE, dm), F32)
    if has_f:
        return _pcall(body, name=name, grid=(T // tm,), in_specs=[row, row, row, row, mod_spec, vec_spec],
                      out_specs=[row, row, stat_spec],
                      out_shape=[_sds((T, dm), F32), _sds((T, dm), MX), st_shape], comm=comm)(x_cur, dh, dx_in, f_prev, mod3, gain)
    return _pcall(body, name=name, grid=(T // tm,), in_specs=[row, row, row, mod_spec, vec_spec],
                  out_specs=[row, stat_spec], out_shape=[_sds((T, dm), F32), st_shape], comm=comm)(x_cur, dh, dx_in, mod3, gain)


def _merge_fwd(z, br_pool, br_mla, *, name, tm=256):
    T = z.shape[0]

    def body(z_ref, bp_ref, bm_ref, o_ref):
        gp = z_ref[:, Z_GP:Z_GP + D]
        gm = z_ref[:, Z_GM:Z_GM + D]
        o_ref[...] = (_sigmoid(gp) * bp_ref[...] + _sigmoid(gm) * bm_ref[...]).astype(o_ref.dtype)

    return _pcall(body, name=name, grid=(T // tm,), in_specs=[_row_spec(tm, Z_W), _row_spec(tm, D), _row_spec(tm, D)],
                  out_specs=_row_spec(tm, D), out_shape=_sds((T, D), MX))(z, br_pool, br_mla)


def _merge_bwd(z, br_pool, br_mla, dmerged, *, name, tm=256):
    T = z.shape[0]

    def body(z_ref, bp_ref, bm_ref, dm_ref, dbp_ref, dbm_ref, dg_ref):
        dm = dm_ref[...]
        sp = _sigmoid(z_ref[:, Z_GP:Z_GP + D])
        sm = _sigmoid(z_ref[:, Z_GM:Z_GM + D])
        dbp_ref[...] = (dm * sp).astype(dbp_ref.dtype)
        dbm_ref[...] = (dm * sm).astype(dbm_ref.dtype)
        dg_ref[:, :D] = (dm * bp_ref[...] * sp * (1.0 - sp)).astype(dg_ref.dtype)
        dg_ref[:, D:] = (dm * bm_ref[...] * sm * (1.0 - sm)).astype(dg_ref.dtype)

    return _pcall(body, name=name, grid=(T // tm,),
                  in_specs=[_row_spec(tm, Z_W), _row_spec(tm, D), _row_spec(tm, D), _row_spec(tm, D)],
                  out_specs=[_row_spec(tm, D), _row_spec(tm, D), _row_spec(tm, 2 * D)],
                  out_shape=[_sds((T, D), MX), _sds((T, D), MX), _sds((T, 2 * D), MX)])(z, br_pool, br_mla, dmerged)


def _shift_down(x, k, row):
    return jnp.where(row >= k, pltpu.roll(x, k, 0), 0.0)


def _shift_up(x, k, row, n):
    return jnp.where(row < n - k, pltpu.roll(x, n - k, 0), 0.0)


def _pool_fwd(z, pool_grp, pool_scale, *, nb, name):
    T = z.shape[0]
    S = T // nb
    blk = pl.BlockSpec((S, LANE), lambda b, g: (b, g))

    def body(u_ref, w_ref, s_ref, pooled_ref, mixed_ref, scaled_ref):
        g = pl.program_id(1)
        u = u_ref[...]
        row = lax.broadcasted_iota(jnp.int32, u.shape, 0)
        s2 = u + _shift_down(u, 1, row)
        s4 = s2 + _shift_down(s2, 2, row)
        s8 = s4 + _shift_down(s4, 4, row)
        s16 = s8 + _shift_down(s8, 8, row)
        win = jnp.where(g == 0, s2, jnp.where(g == 1, s4, jnp.where(g == 2, s8, s16)))
        width = lax.shift_left(jnp.int32(2), g)
        cnt = jnp.minimum(row + 1, width).astype(F32)
        pooled = (win / cnt - u).astype(MX)
        pooled_ref[...] = pooled
        mixed = jnp.dot(pooled, w_ref[0], preferred_element_type=F32)
        mixed_ref[...] = mixed
        scaled_ref[...] = (mixed * s_ref[...]).astype(scaled_ref.dtype)

    return _pcall(body, name=name, grid=(nb, POOL_G),
                  in_specs=[blk, pl.BlockSpec((1, LANE, LANE), lambda b, g: (g, 0, 0)),
                            pl.BlockSpec((1, LANE), lambda b, g: (0, g))],
                  out_specs=[blk, blk, blk],
                  out_shape=[_sds((T, POOL_W), MX), _sds((T, POOL_W), F32), _sds((T, POOL_W), MX)])(z, pool_grp, pool_scale)


def _pool_bwd(dscaled, mixed, pooled, pool_grp, pool_scale, *, nb, name):
    T = dscaled.shape[0]
    S = T // nb
    blk = pl.BlockSpec((S, LANE), lambda g, b: (b, g))

    def body(ds_ref, mixed_ref, pooled_ref, w_ref, s_ref, du_ref, dw_ref, dsc_ref):
        g = pl.program_id(0)
        b = pl.program_id(1)
        ds = ds_ref[...]
        dsc_ref[0] = jnp.sum(ds * mixed_ref[...], axis=0, keepdims=True)
        dmixed = (ds * s_ref[...]).astype(MX)
        dw = lax.dot_general(pooled_ref[...], dmixed, (((0,), (0,)), ((), ())), preferred_element_type=F32)

        @pl.when(b == 0)
        def _():
            dw_ref[0] = dw

        @pl.when(b > 0)
        def _():
            dw_ref[0] += dw
        dpooled = lax.dot_general(dmixed, w_ref[0], (((1,), (1,)), ((), ())), preferred_element_type=F32)
        row = lax.broadcasted_iota(jnp.int32, dpooled.shape, 0)
        width = lax.shift_left(jnp.int32(2), g)
        q = dpooled / jnp.minimum(row + 1, width).astype(F32)
        r2 = q + _shift_up(q, 1, row, S)
        r4 = r2 + _shift_up(r2, 2, row, S)
        r8 = r4 + _shift_up(r4, 4, row, S)
        r16 = r8 + _shift_up(r8, 8, row, S)
        win = jnp.where(g == 0, r2, jnp.where(g == 1, r4, jnp.where(g == 2, r8, r16)))
        du_ref[...] = (win - dpooled).astype(du_ref.dtype)

    return _pcall(body, name=name, grid=(POOL_G, nb),
                  in_specs=[blk, blk, blk, pl.BlockSpec((1, LANE, LANE), lambda g, b: (g, 0, 0)),
                            pl.BlockSpec((1, LANE), lambda g, b: (0, g))],
                  out_specs=[blk, pl.BlockSpec((1, LANE, LANE), lambda g, b: (g, 0, 0)),
                             pl.BlockSpec((1, 1, LANE), lambda g, b: (b, 0, g))],
                  out_shape=[_sds((T, POOL_W), MX), _sds((POOL_G, LANE, LANE), F32), _sds((nb, 1, POOL_W), F32)],
                  )(dscaled, mixed, pooled, pool_grp, pool_scale)


def _lat_fwd(z, q_gain, kv_gain, *, name, tm=256):
    T = z.shape[0]

    def body(z_ref, qg_ref, kg_ref, qn_ref, kvn_ref):
        ql = z_ref[:, Z_QL:Z_QL + QL]
        kv = z_ref[:, Z_KV:Z_KV + KVL]
        qn_ref[...] = (ql * _rsq(ql) * qg_ref[...]).astype(qn_ref.dtype)
        kvn_ref[...] = (kv * _rsq(kv) * kg_ref[...]).astype(kvn_ref.dtype)

    return _pcall(body, name=name, grid=(T // tm,),
                  in_specs=[_row_spec(tm, LAT_W), pl.BlockSpec((1, QL), lambda i: (0, 0)), pl.BlockSpec((1, KVL), lambda i: (0, 0))],
                  out_specs=[_row_spec(tm, QL), _row_spec(tm, KVL)],
                  out_shape=[_sds((T, QL), MX), _sds((T, KVL), MX)])(z, q_gain, kv_gain)


def _lat_bwd(z, dqn, dkvn, q_gain, kv_gain, *, nb, name, tm=256):
    T = z.shape[0]
    tps = (T // nb) // tm

    def norm_bwd(x, dy, gain):
        r = _rsq(x)
        xhat = x * r
        dgain = jnp.sum(dy * xhat, axis=0, keepdims=True)
        dxhat = dy * gain
        return r * (dxhat - xhat * jnp.mean(dxhat * xhat, axis=-1, keepdims=True)), dgain

    def body(z_ref, dq_ref, dkv_ref, qg_ref, kg_ref, dql_ref, dkvl_ref, sq_ref, sk_ref):
        i = pl.program_id(0)
        dql, dqg = norm_bwd(z_ref[:, Z_QL:Z_QL + QL], dq_ref[...], qg_ref[...])
        dkvl, dkg = norm_bwd(z_ref[:, Z_KV:Z_KV + KVL], dkv_ref[...], kg_ref[...])
        dql_ref[...] = dql.astype(dql_ref.dtype)
        dkvl_ref[...] = dkvl.astype(dkvl_ref.dtype)

        @pl.when(i % tps == 0)
        def _():
            sq_ref[...] = jnp.zeros_like(sq_ref)
            sk_ref[...] = jnp.zeros_like(sk_ref)

        sq_ref[0, 0:1, :] += dqg
        sk_ref[0, 0:1, :] += dkg

    return _pcall(body, name=name, grid=(T // tm,),
                  in_specs=[_row_spec(tm, LAT_W), _row_spec(tm, QL), _row_spec(tm, KVL),
                            pl.BlockSpec((1, QL), lambda i: (0, 0)), pl.BlockSpec((1, KVL), lambda i: (0, 0))],
                  out_specs=[_row_spec(tm, QL), _row_spec(tm, KVL),
                             pl.BlockSpec((1, SUBLANE, QL), lambda i: (i // tps, 0, 0)),
                             pl.BlockSpec((1, SUBLANE, KVL), lambda i: (i // tps, 0, 0))],
                  out_shape=[_sds((T, QL), MX), _sds((T, KVL), MX), _sds((nb, SUBLANE, QL), F32), _sds((nb, SUBLANE, KVL), F32)],
                  )(z, dqn, dkvn, q_gain, kv_gain)


def _lane_masks(shape):
    lane = lax.broadcasted_iota(jnp.int32, shape, len(shape) - 1)
    m_n = lane < NOPE
    m_r = jnp.logical_and(lane >= KR_LANE, lane < KR_LANE + ROPE)
    first_half = lane < KR_LANE + ROPE // 2
    return m_n, m_r, first_half


def _rot(y, first_half):
    return jnp.where(first_half, -pltpu.roll(y, LANE - ROPE // 2, 1), pltpu.roll(y, ROPE // 2, 1))


def _rot_t(v, first_half, m_r):
    return jnp.where(m_r, jnp.where(first_half, pltpu.roll(v, LANE - ROPE // 2, 1), -pltpu.roll(v, ROPE // 2, 1)), 0.0)


def _prep_fwd(qp, kvp, z, cos, sin, q_gain, k_gain, *, name, tm=256):
    T = qp.shape[0]
    slab = pl.BlockSpec((tm, LANE), lambda i: (i, 0))
    kr_spec = pl.BlockSpec((tm, LANE), lambda i: (i, Z_KR // LANE))
    vec = pl.BlockSpec((1, LANE), lambda i: (0, 0))

    def body(qp_ref, kvp_ref, kr_ref, cos_ref, sin_ref, qg_ref, kg_ref, q_ref, k_ref, v_ref):
        m_n, m_r, first_half = _lane_masks((tm, LANE))
        c = cos_ref[...]
        s = sin_ref[...]
        qg = qg_ref[...]
        kg = kg_ref[...]
        xr = kr_ref[...]
        rr = lax.rsqrt(jnp.sum(xr * xr, axis=-1, keepdims=True) * (1.0 / ROPE) + EPS)
        yr = xr * rr * kg
        kr = jnp.where(m_r, yr * c + _rot(yr, first_half) * s, 0.0)
        for h in range(NH):
            x = qp_ref[:, h * LANE:(h + 1) * LANE]
            x2 = x * x
            rn = lax.rsqrt(jnp.sum(jnp.where(m_n, x2, 0.0), axis=-1, keepdims=True) * (1.0 / NOPE) + EPS)
            rq = lax.rsqrt(jnp.sum(jnp.where(m_r, x2, 0.0), axis=-1, keepdims=True) * (1.0 / ROPE) + EPS)
            y = x * jnp.where(m_n, rn, jnp.where(m_r, rq, 0.0)) * qg
            q_ref[:, h * LANE:(h + 1) * LANE] = (y * c + _rot(y, first_half) * s).astype(q_ref.dtype)
            xk = kvp_ref[:, h * LANE:(h + 1) * LANE]
            rk = lax.rsqrt(jnp.sum(jnp.where(m_n, xk * xk, 0.0), axis=-1, keepdims=True) * (1.0 / NOPE) + EPS)
            k_ref[:, h * LANE:(h + 1) * LANE] = (jnp.where(m_n, xk * rk * kg, 0.0) + kr).astype(k_ref.dtype)
        v_ref[...] = kvp_ref[:, NH * LANE:].astype(v_ref.dtype)

    return _pcall(body, name=name, grid=(T // tm,),
                  in_specs=[_row_spec(tm, NH * LANE), _row_spec(tm, NH * LANE + NH * NOPE), kr_spec, slab, slab, vec, vec],
                  out_specs=[_row_spec(tm, NH * LANE), _row_spec(tm, NH * LANE), _row_spec(tm, NH * NOPE)],
                  out_shape=[_sds((T, NH * LANE), MX), _sds((T, NH * LANE), MX), _sds((T, NH * NOPE), MX)],
                  )(qp, kvp, z, cos, sin, q_gain, k_gain)


def _prep_bwd(dq, dk, dv, qp, kvp, z, cos, sin, q_gain, k_gain, *, nb, name, tm=256):
    T = qp.shape[0]
    tps = (T // nb) // tm
    slab = pl.BlockSpec((tm, LANE), lambda i: (i, 0))
    kr_spec = pl.BlockSpec((tm, LANE), lambda i: (i, Z_KR // LANE))
    vec = pl.BlockSpec((1, LANE), lambda i: (0, 0))

    def body(dq_ref, dk_ref, dv_ref, qp_ref, kvp_ref, kr_ref, cos_ref, sin_ref, qg_ref, kg_ref,
             dqp_ref, dkvp_ref, dkr_ref, st_ref):
        i = pl.program_id(0)
        m_n, m_r, first_half = _lane_masks((tm, LANE))
        c = cos_ref[...]
        s = sin_ref[...]
        qg = qg_ref[...]
        kg = kg_ref[...]
        dqg = jnp.zeros((1, LANE), F32)
        dkg = jnp.zeros((1, LANE), F32)
        dkr_sum = jnp.zeros((tm, LANE), F32)
        for h in range(NH):
            x = qp_ref[:, h * LANE:(h + 1) * LANE]
            x2 = x * x
            rn = lax.rsqrt(jnp.sum(jnp.where(m_n, x2, 0.0), axis=-1, keepdims=True) * (1.0 / NOPE) + EPS)
            rq = lax.rsqrt(jnp.sum(jnp.where(m_r, x2, 0.0), axis=-1, keepdims=True) * (1.0 / ROPE) + EPS)
            rfac = jnp.where(m_n, rn, jnp.where(m_r, rq, 0.0))
            xhat = x * rfac
            do = dq_ref[:, h * LANE:(h + 1) * LANE]
            dy = do * c + _rot_t(do * s, first_half, m_r)
            dqg = dqg + jnp.sum(dy * xhat, axis=0, keepdims=True)
            dxhat = dy * qg
            t = dxhat * xhat
            mean_n = jnp.sum(jnp.where(m_n, t, 0.0), axis=-1, keepdims=True) * (1.0 / NOPE)
            mean_r = jnp.sum(jnp.where(m_r, t, 0.0), axis=-1, keepdims=True) * (1.0 / ROPE)
            dqp_ref[:, h * LANE:(h + 1) * LANE] = (
                rfac * (dxhat - xhat * jnp.where(m_n, mean_n, jnp.where(m_r, mean_r, 0.0)))).astype(dqp_ref.dtype)

            xk = kvp_ref[:, h * LANE:(h + 1) * LANE]
            rk = lax.rsqrt(jnp.sum(jnp.where(m_n, xk * xk, 0.0), axis=-1, keepdims=True) * (1.0 / NOPE) + EPS)
            khat = jnp.where(m_n, xk * rk, 0.0)
            dko = dk_ref[:, h * LANE:(h + 1) * LANE]
            dkn = jnp.where(m_n, dko, 0.0)
            dkg = dkg + jnp.sum(dkn * khat, axis=0, keepdims=True)
            dkhat = dkn * kg
            mean_k = jnp.sum(dkhat * khat, axis=-1, keepdims=True) * (1.0 / NOPE)
            dkvp_ref[:, h * LANE:(h + 1) * LANE] = jnp.where(m_n, rk * (dkhat - khat * mean_k), 0.0).astype(dkvp_ref.dtype)
            dkr_sum = dkr_sum + jnp.where(m_r, dko, 0.0)
        dkvp_ref[:, NH * LANE:] = dv_ref[...].astype(dkvp_ref.dtype)

        xr = kr_ref[...]
        rr = lax.rsqrt(jnp.sum(xr * xr, axis=-1, keepdims=True) * (1.0 / ROPE) + EPS)
        rhat = xr * rr
        dyr = dkr_sum * c + _rot_t(dkr_sum * s, first_half, m_r)
        dkg = dkg + jnp.sum(dyr * rhat, axis=0, keepdims=True)
        drhat = dyr * kg
        mean_kr = jnp.sum(drhat * rhat, axis=-1, keepdims=True) * (1.0 / ROPE)
        dkr_ref[...] = jnp.where(m_r, rr * (drhat - rhat * mean_kr), 0.0).astype(dkr_ref.dtype)

        @pl.when(i % tps == 0)
        def _():
            st_ref[...] = jnp.zeros_like(st_ref)

        st_ref[0, 0:1, :] += dqg
        st_ref[0, 1:2, :] += dkg

    return _pcall(body, name=name, grid=(T // tm,),
                  in_specs=[_row_spec(tm, NH * LANE), _row_spec(tm, NH * LANE), _row_spec(tm, NH * NOPE),
                            _row_spec(tm, NH * LANE), _row_spec(tm, NH * LANE + NH * NOPE), kr_spec, slab, slab, vec, vec],
                  out_specs=[_row_spec(tm, NH * LANE), _row_spec(tm, NH * LANE + NH * NOPE), slab,
                             pl.BlockSpec((1, SUBLANE, LANE), lambda i: (i // tps, 0, 0))],
                  out_shape=[_sds((T, NH * LANE), MX), _sds((T, NH * LANE + NH * NOPE), MX), _sds((T, LANE), MX),
                             _sds((nb, SUBLANE, LANE), F32)],
                  )(dq, dk, dv, qp, kvp, z, cos, sin, q_gain, k_gain)


def _lower_triangle(t):
    return lax.broadcasted_iota(jnp.int32, (t, t), 1) <= lax.broadcasted_iota(jnp.int32, (t, t), 0)


def _attn_fwd(q, k, v, *, nb, name, tq=512, comm=None):
    T = q.shape[0]
    S = T // nb
    tq = _pick(S, tq)
    nq = S // tq
    tk = tq
    npair = NH // 2

    def body(q_ref, k_ref, v_ref, o_ref, lse_ref):
        qi = pl.program_id(2)
        lane = lax.broadcasted_iota(jnp.int32, (tq, LANE), 1)
        qs = [q_ref[:, hh * LANE:(hh + 1) * LANE] for hh in range(2)]

        def block(j, carry, diagonal):
            k0 = pl.multiple_of(j * tk, tk)
            vb = v_ref[pl.ds(k0, tk), :]
            new = []
            for hh in range(2):
                m, l, acc = carry[hh]
                kb = k_ref[pl.ds(k0, tk), hh * LANE:(hh + 1) * LANE]
                s = lax.dot_general(qs[hh], kb, (((1,), (1,)), ((), ())), preferred_element_type=F32) * ATTN_SCALE
                if diagonal:
                    s = jnp.where(_lower_triangle(tq), s, NEG)
                m_new = jnp.maximum(m, jnp.max(s, axis=-1, keepdims=True))
                p = jnp.exp(s - m_new)
                alpha = jnp.exp(m - m_new)
                l = alpha * l + jnp.sum(p, axis=-1, keepdims=True)
                acc = alpha * acc + jnp.dot(p.astype(MX), vb, preferred_element_type=F32)
                new.append((m_new, l, acc))
            return tuple(new)

        init = tuple((jnp.full((tq, 1), NEG, F32), jnp.zeros((tq, 1), F32), jnp.zeros((tq, LANE), F32)) for _ in range(2))
        carry = lax.fori_loop(0, qi, lambda j, c: block(j, c, False), init)
        (m0, l0, acc0), (m1, l1, acc1) = block(qi, carry, True)
        o_ref[...] = jnp.where(lane < NOPE, acc0 / l0, acc1 / l1).astype(o_ref.dtype)
        lse_ref[...] = jnp.where(lane < NOPE, m0 + jnp.log(l0), m1 + jnp.log(l1))

    return _pcall(body, name=name, grid=(nb, npair, nq),
                  in_specs=[pl.BlockSpec((tq, 2 * LANE), lambda b, p, i: (b * nq + i, p)),
                            pl.BlockSpec((S, 2 * LANE), lambda b, p, i: (b, p)),
                            pl.BlockSpec((S, LANE), lambda b, p, i: (b, p))],
                  out_specs=[pl.BlockSpec((tq, LANE), lambda b, p, i: (b * nq + i, p)),
                             pl.BlockSpec((tq, LANE), lambda b, p, i: (b * nq + i, p))],
                  out_shape=[_sds((T, NH * NOPE), MX), _sds((T, NH * NOPE), F32)], comm=comm)(q, k, v)


def _attn_bwd(q, k, v, o, lse, do, *, nb, name, tq=512, comm=None):
    T = q.shape[0]
    S = T // nb
    tq = _pick(S, tq)
    nq = S // tq
    tk = tq
    npair = NH // 2

    def body(q_ref, k_ref, v_ref, o_ref, lse_ref, do_ref, dq_ref, dk_ref, dv_ref, delta_ref):
        lane = lax.broadcasted_iota(jnp.int32, (tq, LANE), 1)
        first = lane < NOPE
        dq_ref[...] = jnp.zeros_like(dq_ref)

        def delta_step(qi, _):
            q0 = pl.multiple_of(qi * tq, tq)
            prod = do_ref[pl.ds(q0, tq), :] * o_ref[pl.ds(q0, tq), :].astype(F32)
            d0 = jnp.sum(jnp.where(first, prod, 0.0), axis=-1, keepdims=True)
            d1 = jnp.sum(jnp.where(first, 0.0, prod), axis=-1, keepdims=True)
            delta_ref[pl.ds(q0, tq), :] = jnp.where(first, d0, d1)
            return 0

        lax.fori_loop(0, nq, delta_step, 0)

        def kv_step(kj, _):
            k0 = pl.multiple_of(kj * tk, tk)
            kbs = [k_ref[pl.ds(k0, tk), hh * LANE:(hh + 1) * LANE] for hh in range(2)]
            vb = v_ref[pl.ds(k0, tk), :]

            def q_block(qi, carry, diagonal):
                dk0, dk1, dv = carry
                dks = [dk0, dk1]
                q0 = pl.multiple_of(qi * tq, tq)
                dov = do_ref[pl.ds(q0, tq), :]
                lse_v = lse_ref[pl.ds(q0, tq), :]
                delta_v = delta_ref[pl.ds(q0, tq), :]
                for hh in range(2):
                    qh = q_ref[pl.ds(q0, tq), hh * LANE:(hh + 1) * LANE]
                    dob = jnp.where(first if hh == 0 else jnp.logical_not(first), dov, 0.0).astype(MX)
                    s = lax.dot_general(qh, kbs[hh], (((1,), (1,)), ((), ())), preferred_element_type=F32) * ATTN_SCALE
                    p = jnp.exp(s - lse_v[:, hh * NOPE:hh * NOPE + 1])
                    if diagonal:
                        p = jnp.where(_lower_triangle(tq), p, 0.0)
                    dp = lax.dot_general(dob, vb, (((1,), (1,)), ((), ())), preferred_element_type=F32)
                    ds = (p * (dp - delta_v[:, hh * NOPE:hh * NOPE + 1]) * ATTN_SCALE).astype(MX)
                    dks[hh] = dks[hh] + lax.dot_general(ds, qh, (((0,), (0,)), ((), ())), preferred_element_type=F32)
                    dv = dv + lax.dot_general(p.astype(MX), dob, (((0,), (0,)), ((), ())), preferred_element_type=F32)
                    dq_ref[pl.ds(q0, tq), hh * LANE:(hh + 1) * LANE] += jnp.dot(ds, kbs[hh], preferred_element_type=F32)
                return dks[0], dks[1], dv

            zero = jnp.zeros((tk, LANE), F32)
            carry = q_block(kj, (zero, zero, zero), True)
            dk0, dk1, dv = lax.fori_loop(kj + 1, nq, lambda qi, c: q_block(qi, c, False), carry)
            dk_ref[pl.ds(k0, tk), 0:LANE] = dk0
            dk_ref[pl.ds(k0, tk), LANE:2 * LANE] = dk1
            dv_ref[pl.ds(k0, tk), :] = dv
            return 0

        lax.fori_loop(0, nq, kv_step, 0)

    pair256 = pl.BlockSpec((S, 2 * LANE), lambda b, p: (b, p))
    pair128 = pl.BlockSpec((S, LANE), lambda b, p: (b, p))
    return _pcall(body, name=name, grid=(nb, npair),
                  in_specs=[pair256, pair256, pair128, pair128, pair128, pair128],
                  out_specs=[pair256, pair256, pair128],
                  out_shape=[_sds((T, NH * LANE), F32), _sds((T, NH * LANE), F32), _sds((T, NH * NOPE), F32)],
                  scratch=[pltpu.VMEM((S, LANE), F32)], comm=comm)(q, k, v, o, lse, do)


class _NoExchange:
    def __init__(self, weights):
        self.W, self.G = weights, {}

    def rider(self, name):
        return None

    def landed(self, name, outs):
        pass

    def grad(self, key, g):
        self.G[key] = g


def _run(plan, fn, *args, name, **kw):
    rider = plan.rider(name)
    if rider is None:
        return fn(*args, name=name, **kw)
    outs, landed = fn(*args, name=name, comm=rider, **kw)
    plan.landed(name, landed)
    return outs


def _layer_fwd_bwd(x, tgt, mod3, cos, sin, plan, P):
    nb = mod3.shape[0]
    W = plan.W
    h1, gu1, a1 = _run(plan, _ffn_up_normed, x, mod3, P["norm_ffn1"], W["ffn1_in"], sub=0, name="ffn1_up")
    f1, x1, h2 = _mm_resid_norm(a1, W["ffn1_out"], x, mod3, P["norm_mix"], sub=1, coef=0.5, name="ffn1_out")
    z = _run(plan, _mm, h2, W["w_in"], mode="nn", out_dtype=F32, name="mix_in", tn=1664)
    pooled, mixed, scaled = _pool_fwd(z, W["pool_grp"], P["pool_scale"], nb=nb, name="pool_fwd")
    br_pool = _mm(scaled, W["pool_proj"], mode="nn", out_dtype=F32, name="pool_proj")
    qn, kvn = _lat_fwd(z, P["q_a_norm"], P["kv_a_norm"], name="lat_fwd")
    qp = _mm(qn, W["q_up"], mode="nn", out_dtype=F32, name="q_up")
    kvp = _mm(kvn, W["kv_up"], mode="nn", out_dtype=F32, name="kv_up")
    q, k, v = _prep_fwd(qp, kvp, z, cos, sin, P["q_gain"], P["k_gain"], name="prep_fwd")
    attn, lse = _run(plan, _attn_fwd, q, k, v, nb=nb, name="attn_fwd")
    br_mla = _mm(attn, W["mla_proj"], mode="nn", out_dtype=F32, name="mla_proj")
    merged = _merge_fwd(z, br_pool, br_mla, name="merge_fwd")
    mo, x2, h3 = _mm_resid_norm(merged, W["w_out"], x1, mod3, P["norm_ffn2"], sub=2, coef=1.0, name="mix_out")
    gu2, a2 = _ffn_up(h3, W["ffn2_in"], name="ffn2_up")
    dy, df2, st_fin, loss = _mm_loss(a2, W["ffn2_out"], x2, tgt, mod3, name="ffn2_out_loss")

    plan.grad("ffn2_out", _dw(a2, df2, name="d_ffn2_out"))
    dgu2 = _ffn_dact(df2, gu2, W["ffn2_out"], name="ffn2_dact")
    dh3 = _ffn_dh(dgu2, W["ffn2_in"], name="d_ffn2_h")
    plan.grad("ffn2_in", _ffn_dw_in(h3, dgu2, name="d_ffn2_in"))
    dx2, dmo, st3 = _run(plan, _bwd_block, x2, dh3, dy, mo, mod3, P["norm_ffn2"], sub=2, coef=1.0, name="bwd_norm3")
    plan.grad("w_out", _dw(merged, dmo, name="d_mix_out"))
    dmerged = _mm(dmo, W["w_out"], mode="nt", out_dtype=F32, name="d_merged")
    dbr_pool, dbr_mla, dgates = _merge_bwd(z, br_pool, br_mla, dmerged, name="merge_bwd")
    plan.grad("pool_proj", _dw(scaled, dbr_pool, name="d_pool_proj"))
    dscaled = _mm(dbr_pool, W["pool_proj"], mode="nt", out_dtype=F32, name="d_pool_scaled")
    du_pool, d_pool_grp, d_pool_scale = _pool_bwd(dscaled, mixed, pooled, W["pool_grp"], P["pool_scale"], nb=nb, name="pool_bwd")
    plan.grad("mla_proj", _dw(attn, dbr_mla, name="d_mla_proj"))
    dattn = _mm(dbr_mla, W["mla_proj"], mode="nt", out_dtype=F32, name="d_attn")
    dq, dk, dv = _run(plan, _attn_bwd, q, k, v, attn, lse, dattn, nb=nb, name="attn_bwd")
    dqp, dkvp, dkr, st_prep = _prep_bwd(dq, dk, dv, qp, kvp, z, cos, sin, P["q_gain"], P["k_gain"], nb=nb, name="prep_bwd")
    plan.grad("q_up", _dw(qn, dqp, name="d_q_up"))
    dqn = _mm(dqp, W["q_up"], mode="nt", out_dtype=F32, name="d_qn")
    plan.grad("kv_up", _dw(kvn, dkvp, name="d_kv_up"))
    dkvn = _mm(dkvp, W["kv_up"], mode="nt", out_dtype=F32, name="d_kvn")
    dql, dkvl, st_q, st_kv = _lat_bwd(z, dqn, dkvn, P["q_a_norm"], P["kv_a_norm"], nb=nb, name="lat_bwd")
    dz = jnp.concatenate([du_pool, dql, dkvl, dkr, dgates], axis=1)
    plan.grad("w_in", _dw(h2, dz, name="d_mix_in", tn=1664))
    dh2 = _run(plan, _mm, dz, W["w_in"], mode="nt", out_dtype=F32, name="d_mix_h")
    dx1, df1, st2 = _run(plan, _bwd_block, x1, dh2, dx2, f1, mod3, P["norm_mix"], sub=1, coef=0.5, name="bwd_norm2")
    plan.grad("ffn1_out", _run(plan, _dw, a1, df1, name="d_ffn1_out"))
    dgu1 = _run(plan, _ffn_dact, df1, gu1, W["ffn1_out"], name="ffn1_dact")
    plan.grad("ffn1_in", _run(plan, _ffn_dw_in, h1, dgu1, name="d_ffn1_in"))
    dh1 = _run(plan, _ffn_dh, dgu1, W["ffn1_in"], name="d_ffn1_h")
    grad_x, st1 = _run(plan, _bwd_block, x, dh1, dx1, None, mod3, P["norm_ffn1"], sub=0, coef=0.0, name="bwd_norm1")

    return loss, grad_x, (st1, st2, st3, st_fin, st_q, st_kv, st_prep, d_pool_scale), d_pool_grp


def _w_in_to_kernel(w):
    k = w.shape[0]
    zeros = lambda n: jnp.zeros((k, n), w.dtype)
    return jnp.concatenate([w[:, 0:1152], zeros(KR_LANE), w[:, 1152:1184], zeros(LANE - KR_LANE - ROPE), w[:, 1184:]], axis=1)


def _w_in_from_kernel(g):
    return jnp.concatenate([g[:, 0:1152], g[:, Z_KR + KR_LANE:Z_KR + KR_LANE + ROPE], g[:, Z_GP:]], axis=1)


def _q_up_to_kernel(w):
    k = w.shape[0]
    return jnp.pad(w.reshape(k, NH, NOPE + ROPE), ((0, 0), (0, 0), (0, LANE - NOPE - ROPE))).reshape(k, NH * LANE)


def _q_up_from_kernel(g):
    k = g.shape[0]
    return g.reshape(k, NH, LANE)[:, :, :NOPE + ROPE].reshape(k, NH * (NOPE + ROPE))


def _kv_up_to_kernel(w):
    k = w.shape[0]
    w3 = w.reshape(k, NH, 2 * NOPE)
    kpart = jnp.pad(w3[:, :, :NOPE], ((0, 0), (0, 0), (0, LANE - NOPE))).reshape(k, NH * LANE)
    return jnp.concatenate([kpart, w3[:, :, NOPE:].reshape(k, NH * NOPE)], axis=1)


def _kv_up_from_kernel(g):
    k = g.shape[0]
    kpart = g[:, :NH * LANE].reshape(k, NH, LANE)[:, :, :NOPE]
    vpart = g[:, NH * LANE:].reshape(k, NH, NOPE)
    return jnp.concatenate([kpart, vpart], axis=2).reshape(k, NH * 2 * NOPE)


def _gain_slab(nope, rope):
    return jnp.concatenate([nope, rope, jnp.zeros((1, LANE - NOPE - ROPE), nope.dtype)], axis=1)


def _rope_tables(positions):
    inv_freq = 10000.0 ** (-jnp.arange(0, ROPE, 2, dtype=F32) / ROPE)
    ang = positions.astype(F32)[:, None] * inv_freq
    ang = jnp.concatenate([ang, ang], axis=-1)
    t = positions.shape[0]
    cos = jnp.concatenate([jnp.ones((t, KR_LANE), F32), jnp.cos(ang), jnp.ones((t, LANE - KR_LANE - ROPE), F32)], axis=1)
    sin = jnp.concatenate([jnp.zeros((t, KR_LANE), F32), jnp.sin(ang), jnp.zeros((t, LANE - KR_LANE - ROPE), F32)], axis=1)
    return cos, sin


def _coords():
    return lax.axis_index("x"), lax.axis_index("y"), lax.axis_index("c")


HBM_SPEC = pl.BlockSpec(memory_space=pl.ANY)
VMEM_SPEC = pl.BlockSpec(memory_space=pltpu.VMEM)


def _gather8(v, *, name, comm=None):
    r, c = v.shape

    def body(v_ref, out_ref, send_sems, recv_sems, local_sem):
        x, y, cc = _coords()
        me = 4 * x + 2 * y + cc
        mine = pltpu.make_async_copy(v_ref, out_ref.at[me], local_sem)
        mine.start()
        copies = []
        for kk in range(1, N_DEV):
            peer = (x ^ (kk >> 2), y ^ ((kk >> 1) & 1), cc ^ (kk & 1))
            cp = pltpu.make_async_remote_copy(src_ref=v_ref, dst_ref=out_ref.at[me], send_sem=send_sems.at[kk - 1],
                                              recv_sem=recv_sems.at[kk - 1], device_id=peer, device_id_type=MESH)
            cp.start()
            copies.append(cp)
        for kk in range(1, N_DEV):
            peer_slot = me ^ kk
            pltpu.make_async_remote_copy(src_ref=v_ref, dst_ref=out_ref.at[peer_slot], send_sem=send_sems.at[kk - 1],
                                         recv_sem=recv_sems.at[kk - 1], device_id=(x, y, cc), device_id_type=MESH).wait_recv()
        for cp in copies:
            cp.wait_send()
        mine.wait()

    return _pcall(body, name=name, out_shape=_sds((N_DEV, r, c), v.dtype), in_specs=[VMEM_SPEC], out_specs=VMEM_SPEC,
                  scratch=[pltpu.SemaphoreType.DMA((N_DEV - 1,)), pltpu.SemaphoreType.DMA((N_DEV - 1,)), pltpu.SemaphoreType.DMA],
                  comm=comm)(v)


CHIP_RELS = ((1, 0), (0, 1), (1, 1))


def _comm_only(comm, *, name):
    _, outs = _pcall(lambda: None, name=name, out_shape=[], in_specs=[], out_specs=[], comm=comm)()
    return outs


def _gather_comm(shards):
    n = len(shards)

    def ici(ins, outs, sems, a, j, x, y, cc):
        half = ins[a].shape[0] // 2
        mine = pl.ds(cc * half, half)
        dx, dy = CHIP_RELS[j]
        return pltpu.make_async_remote_copy(src_ref=ins[a].at[mine], dst_ref=outs[a].at[2 * x + y, mine],
                                            send_sem=sems[0].at[a, j], recv_sem=sems[1].at[a, j],
                                            device_id=(x ^ dx, y ^ dy, cc), device_id_type=MESH)

    def d2d(ins, outs, sems, a, j, x, y, cc, half_of):
        half = ins[a].shape[0] // 2
        dx, dy = CHIP_RELS[j]
        landed = outs[a].at[2 * (x ^ dx) + (y ^ dy), pl.ds(half_of * half, half)]
        return pltpu.make_async_remote_copy(src_ref=landed, dst_ref=landed, send_sem=sems[2].at[a, j], recv_sem=sems[3].at[a, j],
                                            device_id=(x, y, 1 - cc), device_id_type=MESH)

    def own(ins, outs, sems, a, x, y):
        return pltpu.make_async_copy(ins[a], outs[a].at[2 * x + y], sems[4].at[a])

    def start(ins, outs, sems):
        x, y, cc = _coords()
        for a in range(n):
            own(ins, outs, sems, a, x, y).start()
            for j in range(3):
                ici(ins, outs, sems, a, j, x, y, cc).start()

    def finish(ins, outs, sems):
        x, y, cc = _coords()
        for a in range(n):
            for j in range(3):
                ici(ins, outs, sems, a, j, x, y, cc).wait_recv()
                d2d(ins, outs, sems, a, j, x, y, cc, cc).start()
        for a in range(n):
            for j in range(3):
                d2d(ins, outs, sems, a, j, x, y, cc, 1 - cc).wait_recv()
        for a in range(n):
            for j in range(3):
                ici(ins, outs, sems, a, j, x, y, cc).wait_send()
                d2d(ins, outs, sems, a, j, x, y, cc, cc).wait_send()
            own(ins, outs, sems, a, x, y).wait()

    dma = pltpu.SemaphoreType.DMA
    return _Comm(shards, [_sds((N_CHIP,) + s.shape, s.dtype) for s in shards],
                 [dma((n, 3)), dma((n, 3)), dma((n, 3)), dma((n, 3)), dma((n,))], start, finish)


def _swap_comm(parts):
    n = len(parts)

    def copy(ins, outs, sems, a):
        x, y, cc = _coords()
        half = ins[a].shape[1] // 2
        return pltpu.make_async_remote_copy(src_ref=ins[a].at[:, pl.ds((1 - cc) * half, half)], dst_ref=outs[a],
                                            send_sem=sems[0].at[a], recv_sem=sems[1].at[a], device_id=(x, y, 1 - cc),
                                            device_id_type=MESH)

    def start(ins, outs, sems):
        for a in range(n):
            copy(ins, outs, sems, a).start()

    def finish(ins, outs, sems):
        for a in range(n):
            copy(ins, outs, sems, a).wait()

    dma = pltpu.SemaphoreType.DMA
    return _Comm(parts, [_sds((p.shape[0], p.shape[1] // 2, p.shape[2]), p.dtype) for p in parts], [dma((n,)), dma((n,))],
                 start, finish)


def _add_half(full, other, cidx, *, name):
    nch, r, c = full.shape
    half = r // 2
    tr = _pick_rows(half)
    nbk = half // tr
    grid_spec = pltpu.PrefetchScalarGridSpec(
        num_scalar_prefetch=1, grid=(nch, nbk),
        in_specs=[pl.BlockSpec((1, tr, c), lambda j, i, cref: (j, cref[0] * nbk + i, 0)),
                  pl.BlockSpec((1, tr, c), lambda j, i, cref: (j, i, 0))],
        out_specs=pl.BlockSpec((1, tr, c), lambda j, i, cref: (j, i, 0)))

    def body(cref, a_ref, b_ref, o_ref):
        o_ref[...] = (a_ref[...] + b_ref[...]).astype(o_ref.dtype)

    return _pcall(body, name=name, out_shape=_sds((nch, half, c), MX), grid_spec=grid_spec)(cidx, full, other)


def _pick_rows(rows, target=512):
    best = None
    for t in range(16, min(rows, target) + 1, 16):
        if rows % t == 0:
            best = t
    return rows if best is None else best


def _exchange_comm(parts):
    n = len(parts)

    def send(ins, outs, sems, a, j, x, y, cc):
        dx, dy = CHIP_RELS[j]
        return pltpu.make_async_remote_copy(src_ref=ins[a].at[2 * (x ^ dx) + (y ^ dy)], dst_ref=outs[a].at[2 * x + y],
                                            send_sem=sems[0].at[a, j], recv_sem=sems[1].at[a, j],
                                            device_id=(x ^ dx, y ^ dy, cc), device_id_type=MESH)

    def landing(ins, outs, sems, a, j, x, y, cc):
        dx, dy = CHIP_RELS[j]
        peer_chip = 2 * (x ^ dx) + (y ^ dy)
        return pltpu.make_async_remote_copy(src_ref=ins[a].at[peer_chip], dst_ref=outs[a].at[peer_chip], send_sem=sems[0].at[a, j],
                                            recv_sem=sems[1].at[a, j], device_id=(x, y, cc), device_id_type=MESH)

    def own(ins, outs, sems, a, x, y):
        return pltpu.make_async_copy(ins[a].at[2 * x + y], outs[a].at[2 * x + y], sems[2].at[a])

    def start(ins, outs, sems):
        x, y, cc = _coords()
        for a in range(n):
            own(ins, outs, sems, a, x, y).start()
            for j in range(3):
                send(ins, outs, sems, a, j, x, y, cc).start()

    def finish(ins, outs, sems):
        x, y, cc = _coords()
        for a in range(n):
            for j in range(3):
                landing(ins, outs, sems, a, j, x, y, cc).wait_recv()
        for a in range(n):
            for j in range(3):
                send(ins, outs, sems, a, j, x, y, cc).wait_send()
            own(ins, outs, sems, a, x, y).wait()

    dma = pltpu.SemaphoreType.DMA
    return _Comm(parts, [_sds(p.shape, p.dtype) for p in parts], [dma((n, 3)), dma((n, 3)), dma((n,))], start, finish)


def _sum_chips(q, cidx, *, name):
    nch, h, c = q.shape
    tr = _pick_rows(h)
    nbk = h // tr
    grid_spec = pltpu.PrefetchScalarGridSpec(
        num_scalar_prefetch=1, grid=(nbk,),
        in_specs=[pl.BlockSpec((nch, tr, c), lambda i, cref: (0, i, 0))],
        out_specs=pl.BlockSpec((tr, c), lambda i, cref: (cref[0] * nbk + i, 0)))

    def body(cref, q_ref, o_ref):
        acc = q_ref[0].astype(F32) + q_ref[1].astype(F32)
        acc = acc + q_ref[2].astype(F32)
        o_ref[...] = acc + q_ref[3].astype(F32)

    return _pcall(body, name=name, out_shape=_sds((2 * h, c), F32), grid_spec=grid_spec)(cidx, q)


def _join_comm(fulls):
    n = len(fulls)

    def half_copy(outs, sems, a, which):
        x, y, cc = _coords()
        h = outs[a].shape[0] // 2
        rows = outs[a].at[pl.ds((cc if which == 0 else 1 - cc) * h, h)]
        return pltpu.make_async_remote_copy(src_ref=rows, dst_ref=rows, send_sem=sems[0].at[a], recv_sem=sems[1].at[a],
                                            device_id=(x, y, 1 - cc), device_id_type=MESH)

    def start(ins, outs, sems):
        for a in range(n):
            half_copy(outs, sems, a, 0).start()

    def finish(ins, outs, sems):
        for a in range(n):
            half_copy(outs, sems, a, 1).wait_recv()
        for a in range(n):
            half_copy(outs, sems, a, 0).wait_send()

    dma = pltpu.SemaphoreType.DMA
    return _Comm(fulls, [_sds(p.shape, p.dtype) for p in fulls], [dma((n,)), dma((n,))], start, finish,
                 aliases={a: a for a in range(n)})


def _ada_prologue(c, w, b, *, name, comm=None):
    nb, dm = c.shape
    n = w.shape[1]

    def body(c_ref, w_ref, b_ref, call_ref, land_ref, cpad, mod_s, g_send, g_recv, m_send, m_recv):
        x, y, cc = _coords()
        me = 4 * x + 2 * y + cc
        chip = 2 * x + y
        cpad[...] = jnp.zeros_like(cpad)
        cpad[0:nb, :] = c_ref[...]
        copies = []
        for kk in range(1, N_DEV):
            peer = (x ^ (kk >> 2), y ^ ((kk >> 1) & 1), cc ^ (kk & 1))
            cp = pltpu.make_async_remote_copy(src_ref=cpad, dst_ref=call_ref.at[me], send_sem=g_send.at[kk - 1],
                                              recv_sem=g_recv.at[kk - 1], device_id=peer, device_id_type=MESH)
            cp.start()
            copies.append(cp)
        call_ref[me] = cpad[...]
        for kk in range(1, N_DEV):
            pltpu.make_async_remote_copy(src_ref=cpad, dst_ref=call_ref.at[me ^ kk], send_sem=g_send.at[kk - 1],
                                         recv_sem=g_recv.at[kk - 1], device_id=(x, y, cc), device_id_type=MESH).wait_recv()
        cv = call_ref[...].reshape(N_DEV * SUBLANE, dm)
        act = (cv * _sigmoid(cv)).astype(MX)
        mod = jnp.dot(act, w_ref[...].astype(MX), preferred_element_type=F32) + b_ref[...]
        mod_s[...] = mod.reshape(N_DEV, SUBLANE, n)
        for j, (dx, dy) in enumerate(CHIP_RELS):
            cp = pltpu.make_async_remote_copy(src_ref=mod_s.at[4 * (x ^ dx) + 2 * (y ^ dy) + cc], dst_ref=land_ref.at[chip],
                                              send_sem=m_send.at[j], recv_sem=m_recv.at[j],
                                              device_id=(x ^ dx, y ^ dy, cc), device_id_type=MESH)
            cp.start()
            copies.append(cp)
        land_ref[chip] = mod_s[me]
        for j, (dx, dy) in enumerate(CHIP_RELS):
            pltpu.make_async_remote_copy(src_ref=mod_s.at[me], dst_ref=land_ref.at[2 * (x ^ dx) + (y ^ dy)], send_sem=m_send.at[j],
                                         recv_sem=m_recv.at[j], device_id=(x, y, cc), device_id_type=MESH).wait_recv()
        for cp in copies:
            cp.wait_send()

    dma = pltpu.SemaphoreType.DMA
    return _pcall(body, name=name, in_specs=[VMEM_SPEC] * 3, out_specs=[VMEM_SPEC] * 2,
                  out_shape=[_sds((N_DEV, SUBLANE, dm), F32), _sds((N_CHIP, SUBLANE, n), F32)],
                  scratch=[pltpu.VMEM((SUBLANE, dm), F32), pltpu.VMEM((N_DEV, SUBLANE, n), F32),
                           dma((N_DEV - 1,)), dma((N_DEV - 1,)), dma((3,)), dma((3,))], comm=comm)(c, w, b)


def _ada_bwd(c_all, dmod_cols, *, name):
    m, kdim = c_all.shape
    n = dmod_cols.shape[1]
    tn = _pick(n, 1152)

    def body(c_ref, d_ref, o_ref):
        cv = c_ref[...]
        act = (cv * _sigmoid(cv)).astype(MX)
        o_ref[...] = lax.dot_general(act, d_ref[...].astype(MX), (((0,), (0,)), ((), ())), preferred_element_type=F32)

    return _pcall(body, name=name, out_shape=_sds((kdim, n), F32), grid=(n // tn,),
                  in_specs=[pl.BlockSpec((m, kdim), lambda j: (0, 0)), pl.BlockSpec((m, tn), lambda j: (0, j))],
                  out_specs=pl.BlockSpec((kdim, tn), lambda j: (0, j)))(c_all, dmod_cols)


SLAB_W = 1024
RED_ROWS = 8


LOSS_LANE = SLAB_W - LANE


def _pack_stats(st1, st2, st3, st_fin, st_q, st_kv, st_prep, d_pool_scale, loss8, *, name):
    nb = st1.shape[0]
    dm_rows = -(-9 * nb // SUBLANE) * SUBLANE

    def body(s1, s2, s3, sf, sq, skv, sp, sps, loss_ref, o_ref):
        o_ref[...] = jnp.zeros_like(o_ref)
        for s in range(nb):
            rows = [s1[s, 0:1, :], s1[s, 1:2, :], s2[s, 3:4, :], s2[s, 0:1, :], s2[s, 1:2, :], s3[s, 3:4, :], s3[s, 0:1, :],
                    s3[s, 1:2, :], sf[s, 0:1, :]]
            for k, row in enumerate(rows):
                o_ref[9 * s + k:9 * s + k + 1, :] = row

        def over_seq(ref, r):
            acc = ref[0, r:r + 1, :]
            for s in range(1, nb):
                acc = acc + ref[s, r:r + 1, :]
            return acc

        o_ref[dm_rows + 0:dm_rows + 1, :] = over_seq(s1, 2)
        o_ref[dm_rows + 1:dm_rows + 2, :] = over_seq(s2, 2)
        o_ref[dm_rows + 2:dm_rows + 3, :] = over_seq(s3, 2)
        o_ref[dm_rows + 3:dm_rows + 4, 0:POOL_W] = over_seq(sps, 0)
        o_ref[dm_rows + 4:dm_rows + 5, 0:QL] = over_seq(sq, 0)
        o_ref[dm_rows + 5:dm_rows + 6, 0:KVL] = over_seq(skv, 0)
        o_ref[dm_rows + 6:dm_rows + 7, 0:LANE] = over_seq(sp, 0)
        o_ref[dm_rows + 7:dm_rows + 8, 0:LANE] = over_seq(sp, 1)
        o_ref[dm_rows + 7:dm_rows + 8, LOSS_LANE:] = loss_ref[0:1, :]

    return _pcall(body, name=name, out_shape=_sds((dm_rows + RED_ROWS, SLAB_W), F32), in_specs=[VMEM_SPEC] * 9,
                  out_specs=VMEM_SPEC)(st1, st2, st3, st_fin, st_q, st_kv, st_prep, d_pool_scale, loss8)


def _small_allreduce(slab, pool, *, name, comm=None):
    rows, w = slab.shape
    dm = rows - RED_ROWS
    prow, pw = pool.shape
    crow = RED_ROWS + 2 * dm

    def body(slab_ref, pool_ref, red_ref, dmod_ref, ptot_ref, sib_slab, sib_pool, chip_slab, chip_pool, land_slab, land_pool,
             a_send, a_recv, b_send, b_recv):
        x, y, cc = _coords()
        chip = 2 * x + y
        sib = (x, y, 1 - cc)
        to_sib = [pltpu.make_async_remote_copy(src_ref=slab_ref, dst_ref=sib_slab, send_sem=a_send.at[0], recv_sem=a_recv.at[0],
                                               device_id=sib, device_id_type=MESH),
                  pltpu.make_async_remote_copy(src_ref=pool_ref, dst_ref=sib_pool, send_sem=a_send.at[1], recv_sem=a_recv.at[1],
                                               device_id=sib, device_id_type=MESH)]
        for cp in to_sib:
            cp.start()
        for cp in to_sib:
            cp.wait()
        mine_dm, theirs_dm = slab_ref[0:dm, :], sib_slab[0:dm, :]
        chip_slab[0:RED_ROWS, :] = slab_ref[dm:, :] + sib_slab[dm:, :]
        chip_slab[RED_ROWS:RED_ROWS + dm, :] = jnp.where(cc == 0, mine_dm, theirs_dm)
        chip_slab[RED_ROWS + dm:, :] = jnp.where(cc == 0, theirs_dm, mine_dm)
        chip_pool[...] = pool_ref[...] + sib_pool[...]

        sends = []
        for j, (dx, dy) in enumerate(CHIP_RELS):
            peer = (x ^ dx, y ^ dy, cc)
            sends.append(pltpu.make_async_remote_copy(src_ref=chip_slab, dst_ref=land_slab.at[chip], send_sem=b_send.at[j, 0],
                                                      recv_sem=b_recv.at[j, 0], device_id=peer, device_id_type=MESH))
            sends.append(pltpu.make_async_remote_copy(src_ref=chip_pool, dst_ref=land_pool.at[chip], send_sem=b_send.at[j, 1],
                                                      recv_sem=b_recv.at[j, 1], device_id=peer, device_id_type=MESH))
        for cp in sends:
            cp.start()
        land_slab[chip] = chip_slab[...]
        land_pool[chip] = chip_pool[...]
        for j, (dx, dy) in enumerate(CHIP_RELS):
            peer_chip = 2 * (x ^ dx) + (y ^ dy)
            pltpu.make_async_remote_copy(src_ref=chip_slab, dst_ref=land_slab.at[peer_chip], send_sem=b_send.at[j, 0],
                                         recv_sem=b_recv.at[j, 0], device_id=(x, y, cc), device_id_type=MESH).wait_recv()
            pltpu.make_async_remote_copy(src_ref=chip_pool, dst_ref=land_pool.at[peer_chip], send_sem=b_send.at[j, 1],
                                         recv_sem=b_recv.at[j, 1], device_id=(x, y, cc), device_id_type=MESH).wait_recv()
        red = land_slab[0, 0:RED_ROWS, :]
        ptot = land_pool[0]
        for ch in range(1, N_CHIP):
            red = red + land_slab[ch, 0:RED_ROWS, :]
            ptot = ptot + land_pool[ch]
        red_ref[...] = red
        ptot_ref[...] = ptot
        for ch in range(N_CHIP):
            dmod_ref[2 * dm * ch:2 * dm * (ch + 1), :] = land_slab[ch, RED_ROWS:, :]
        for cp in sends:
            cp.wait_send()

    dma = pltpu.SemaphoreType.DMA
    return _pcall(body, name=name, in_specs=[VMEM_SPEC, VMEM_SPEC], out_specs=[VMEM_SPEC] * 3,
                  out_shape=[_sds((RED_ROWS, w), F32), _sds((N_DEV * dm, w), F32), _sds((prow, pw), F32)],
                  scratch=[pltpu.VMEM((rows, w), F32), pltpu.VMEM((prow, pw), F32), pltpu.VMEM((crow, w), F32),
                           pltpu.VMEM((prow, pw), F32), pltpu.VMEM((N_CHIP, crow, w), F32), pltpu.VMEM((N_CHIP, prow, pw), F32),
                           dma((2,)), dma((2,)), dma((3, 2)), dma((3, 2))], comm=comm)(slab, pool)


def _adamw_math(w, g, m, v):
    mn = ADAM_B1 * m + (1.0 - ADAM_B1) * g
    vn = ADAM_B2 * v + (1.0 - ADAM_B2) * (g * g)
    bc1 = 1.0 / (1.0 - ADAM_B1 ** ADAM_STEP)
    bc2 = 1.0 / (1.0 - ADAM_B2 ** ADAM_STEP)
    return -ADAM_LR * ((mn * bc1) / (jnp.sqrt(vn * bc2) + ADAM_EPS) + ADAM_WD * w), mn, vn


def _small_update(red, dmod_all, pool_total, nb, params, *, name):
    names = list(SMALL)
    dm = dmod_all.shape[0] // N_DEV

    def grad_of(nm, red_ref, dmod_ref, ptot_ref):
        if nm == "b_ada":
            acc = None
            for d in range(N_DEV):
                for s in range(nb):
                    blk = dmod_ref[d * dm + 9 * s:d * dm + 9 * s + 9, :]
                    acc = blk if acc is None else acc + blk
            return jnp.concatenate([acc[k:k + 1, :] for k in range(9)], axis=1)
        if nm == "pool_grp":
            return ptot_ref[...]
        row, lo, n = {"norm_ffn1": (0, 0, D), "norm_mix": (1, 0, D), "norm_ffn2": (2, 0, D), "pool_scale": (3, 0, POOL_W),
                      "q_a_norm": (4, 0, QL), "kv_a_norm": (5, 0, KVL), "q_norm_nope": (6, 0, NOPE),
                      "q_norm_rope": (6, NOPE, ROPE), "k_norm_nope": (7, 0, NOPE), "k_norm_rope": (7, KR_LANE, ROPE)}[nm]
        return red_ref[row:row + 1, lo:lo + n]

    def body(*refs):
        red_ref, dmod_ref, ptot_ref = refs[:3]
        ins = refs[3:3 + 3 * len(names)]
        outs = refs[3 + 3 * len(names):]
        outs[4 * len(names)][...] = red_ref[RED_ROWS - 1:RED_ROWS, LOSS_LANE:]
        for i, nm in enumerate(names):
            g = grad_of(nm, red_ref, dmod_ref, ptot_ref)
            d, mn, vn = _adamw_math(ins[3 * i][...], g, ins[3 * i + 1][...], ins[3 * i + 2][...])
            outs[4 * i][...] = g
            outs[4 * i + 1][...] = d
            outs[4 * i + 2][...] = mn
            outs[4 * i + 3][...] = vn

    flat_in = [a for nm in names for a in params[nm]]
    out_shape = [_sds(params[nm][0].shape, F32) for nm in names for _ in range(4)] + [_sds((1, LANE), F32)]
    res = _pcall(body, name=name, in_specs=[VMEM_SPEC] * (3 + len(flat_in)), out_specs=[VMEM_SPEC] * len(out_shape),
                 out_shape=out_shape)(red, dmod_all, pool_total, *flat_in)
    return {nm: tuple(res[4 * i:4 * i + 4]) for i, nm in enumerate(names)}, res[-1]


def _adamw(w, g, m, v, *, name, comm=None):
    r, c = w.shape
    tr = _pick_rows(r, 256)
    tc = c if tr < r else _pick(c, 256)
    spec = pl.BlockSpec((tr, tc), lambda i, j: (i, j))
    bc1 = 1.0 / (1.0 - ADAM_B1 ** ADAM_STEP)
    bc2 = 1.0 / (1.0 - ADAM_B2 ** ADAM_STEP)

    def body(w_ref, g_ref, m_ref, v_ref, d_ref, mo_ref, vo_ref):
        gv = g_ref[...]
        mn = ADAM_B1 * m_ref[...] + (1.0 - ADAM_B1) * gv
        vn = ADAM_B2 * v_ref[...] + (1.0 - ADAM_B2) * (gv * gv)
        mo_ref[...] = mn
        vo_ref[...] = vn
        d_ref[...] = -ADAM_LR * ((mn * bc1) / (jnp.sqrt(vn * bc2) + ADAM_EPS) + ADAM_WD * w_ref[...])

    out = _sds((r, c), F32)
    return _pcall(body, name=name, out_shape=[out, out, out], grid=(r // tr, c // tc), in_specs=[spec] * 4, out_specs=[spec] * 3,
                  comm=comm)(w, g, m, v)


BIG = ("w_ffn1_in", "w_ffn1_out", "w_in", "w_pool_proj", "w_q_up", "w_kv_up", "w_mla_proj", "w_out", "w_ffn2_in", "w_ffn2_out")
ROW_SHARDED = ("w_ffn1_out", "w_out", "w_ffn2_out")
KERNEL_NAME = {"w_ffn1_in": "ffn1_in", "w_ffn1_out": "ffn1_out", "w_in": "w_in", "w_pool_proj": "pool_proj", "w_q_up": "q_up",
               "w_kv_up": "kv_up", "w_mla_proj": "mla_proj", "w_out": "w_out", "w_ffn2_in": "ffn2_in", "w_ffn2_out": "ffn2_out"}
WEIGHTS = ("w_ada", "b_ada", "norm_ffn1", "w_ffn1_in", "w_ffn1_out", "norm_mix", "w_in", "pool_grp", "pool_scale", "w_pool_proj",
           "q_a_norm", "w_q_up", "kv_a_norm", "w_kv_up", "q_norm_nope", "q_norm_rope", "k_norm_nope", "k_norm_rope", "w_mla_proj",
           "w_out", "norm_ffn2", "w_ffn2_in", "w_ffn2_out")
SMALL = ("b_ada", "norm_ffn1", "norm_mix", "pool_grp", "pool_scale", "q_a_norm", "kv_a_norm", "q_norm_nope", "q_norm_rope",
         "k_norm_nope", "k_norm_rope", "norm_ffn2")
SMALL_SEQ = tuple(n for n in SMALL if n != "pool_grp")
SLAB_W = 1024


def _assemble(name, stacked):
    if name in ROW_SHARDED:
        return stacked.reshape(stacked.shape[0] * stacked.shape[1], stacked.shape[2])
    return jnp.transpose(stacked, (1, 0, 2)).reshape(stacked.shape[1], stacked.shape[0] * stacked.shape[2])


def _split(name, full):
    if name in ROW_SHARDED:
        return full.reshape(N_CHIP, full.shape[0] // N_CHIP, full.shape[1])
    return jnp.transpose(full.reshape(full.shape[0], N_CHIP, full.shape[1] // N_CHIP), (1, 0, 2))


GU = ("w_ffn1_in", "w_ffn2_in")
NARROW = ("w_in", "w_q_up")


def _to_rows(v, width=SLAB_W):
    flat = v.reshape(-1)
    rows = -(-flat.shape[0] // width)
    return jnp.pad(flat, (0, rows * width - flat.shape[0])).reshape(rows, width)


def _pack_small(parts, names=SMALL):
    rows, spans, at = [], {}, 0
    for name in names:
        r = _to_rows(parts[name])
        spans[name] = (at, parts[name].size, parts[name].shape)
        rows.append(r)
        at += r.shape[0]
    pad = (-at) % SUBLANE
    if pad:
        rows.append(jnp.zeros((pad, SLAB_W), F32))
    return jnp.concatenate(rows, axis=0), spans


def _unpack_small(slab, spans):
    out = {}
    for name, (at, size, shape) in spans.items():
        nrow = -(-size // SLAB_W)
        out[name] = slab[at:at + nrow].reshape(-1)[:size].reshape(shape)
    return out


GATHER_ON = {"ada_prologue": ("w_ffn1_in",),
             "ffn1_up": ("w_ffn1_out", "w_in"),
             "mix_in": ("w_pool_proj", "w_q_up", "w_kv_up", "w_mla_proj", "w_out"),
             "attn_fwd": ("w_ffn2_in", "w_ffn2_out")}
SWAP_ON = {"bwd_norm3": ("w_ffn2_out", "w_ffn2_in"),
           "d_mix_h": ("w_out", "w_pool_proj", "w_mla_proj", "w_q_up", "w_kv_up", "w_in"),
           "ffn1_dact": ("w_ffn1_out",),
           "d_ffn1_h": ("w_ffn1_in",)}
EXCHANGE_ON = {"attn_bwd": ("bwd_norm3", ("w_ffn2_out", "w_ffn2_in")),
               "bwd_norm2": ("d_mix_h", ("w_out", "w_pool_proj", "w_mla_proj", "w_q_up", "w_kv_up")),
               "d_ffn1_out": ("d_mix_h", ("w_in",)),
               "d_ffn1_in": ("ffn1_dact", ("w_ffn1_out",)),
               "bwd_norm1": ("d_ffn1_h", ("w_ffn1_in",))}


class _ExchangePlan:
    def __init__(self, shards, cidx):
        self.shards, self.cidx = shards, cidx
        self.W, self.G, self.parts, self.pre, self.reduced = {}, {}, {}, {}, {}

    def grad(self, key, g):
        self.G[key] = g

    def rider(self, name):
        if name in GATHER_ON:
            return _gather_comm([self.shards[n] for n in GATHER_ON[name]])
        if name in SWAP_ON:
            self.parts[name] = [self._stacked(n) for n in SWAP_ON[name]]
            return _swap_comm(self.parts[name])
        if name in EXCHANGE_ON:
            swap_host, names = EXCHANGE_ON[name]
            return _exchange_comm([self.pre[swap_host][SWAP_ON[swap_host].index(n)] for n in names])
        return None

    def landed(self, name, outs):
        if name in GATHER_ON:
            for n, g in zip(GATHER_ON[name], outs):
                self.W[KERNEL_NAME[n]] = self._to_kernel(n, g)
        elif name in SWAP_ON:
            self.pre[name] = [_add_half(p, o, self.cidx, name="rs_add_" + KERNEL_NAME[n])
                              for n, p, o in zip(SWAP_ON[name], self.parts[name], outs)]
        elif name in EXCHANGE_ON:
            for n, q in zip(EXCHANGE_ON[name][1], outs):
                self.reduced[n] = _sum_chips(q, self.cidx, name="rs_sum_" + KERNEL_NAME[n])

    @staticmethod
    def _to_kernel(n, stacked):
        if n in GU:
            return stacked
        full = _assemble(n, stacked)
        return {"w_in": _w_in_to_kernel, "w_q_up": _q_up_to_kernel, "w_kv_up": _kv_up_to_kernel}.get(n, lambda w: w)(full)

    def _stacked(self, n):
        g = self.G[KERNEL_NAME[n]]
        if n in GU:
            return g
        full = {"w_in": _w_in_from_kernel, "w_q_up": _q_up_from_kernel, "w_kv_up": _kv_up_from_kernel}.get(n, lambda w: w)(g)
        return _split(n, full)


def kernel(x, c, positions, w_ada, b_ada, norm_ffn1, w_ffn1_in, w_ffn1_out, norm_mix, w_in, pool_grp, pool_scale, w_pool_proj, q_a_norm, w_q_up, kv_a_norm, w_kv_up, q_norm_nope, q_norm_rope, k_norm_nope, k_norm_rope, w_mla_proj, w_out, norm_ffn2, w_ffn2_in, w_ffn2_out, loss_target, m_w_ada, m_b_ada, m_norm_ffn1, m_w_ffn1_in, m_w_ffn1_out, m_norm_mix, m_w_in, m_pool_grp, m_pool_scale, m_w_pool_proj, m_q_a_norm, m_w_q_up, m_kv_a_norm, m_w_kv_up, m_q_norm_nope, m_q_norm_rope, m_k_norm_nope, m_k_norm_rope, m_w_mla_proj, m_w_out, m_norm_ffn2, m_w_ffn2_in, m_w_ffn2_out, v_w_ada, v_b_ada, v_norm_ffn1, v_w_ffn1_in, v_w_ffn1_out, v_norm_mix, v_w_in, v_pool_grp, v_pool_scale, v_w_pool_proj, v_q_a_norm, v_w_q_up, v_kv_a_norm, v_w_kv_up, v_q_norm_nope, v_q_norm_rope, v_k_norm_nope, v_k_norm_rope, v_w_mla_proj, v_w_out, v_norm_ffn2, v_w_ffn2_in, v_w_ffn2_out):
    args = dict(locals())
    wts = {n: args[n][0] for n in WEIGHTS}
    mom = {n: args["m_" + n][0] for n in WEIGHTS}
    var = {n: args["v_" + n][0] for n in WEIGHTS}
    nb, seq, dm = x.shape
    tokens = nb * seq
    xi, yi, ci = _coords()
    chip = 2 * xi + yi
    dev = 4 * xi + 2 * yi + ci

    cidx = ci.astype(jnp.int32).reshape(1)
    plan = _ExchangePlan({n: wts[n].astype(MX) for n in BIG}, cidx)
    plan.W["pool_grp"] = wts["pool_grp"].astype(MX)

    ncol = w_ada.shape[2]
    b_cols = lax.dynamic_slice_in_dim(wts["b_ada"].reshape(1, -1), chip * ncol, ncol, axis=1)
    c_slots, mod_slots = _run(plan, _ada_prologue, c, wts["w_ada"], b_cols, name="ada_prologue")
    c_all = c_slots[:, :nb].reshape(N_DEV * nb, dm)
    mod3 = jnp.transpose(mod_slots[:, :nb], (1, 0, 2)).reshape(nb, 9, dm)
    P = {"norm_ffn1": wts["norm_ffn1"].reshape(1, dm), "norm_mix": wts["norm_mix"].reshape(1, dm),
         "norm_ffn2": wts["norm_ffn2"].reshape(1, dm), "pool_scale": wts["pool_scale"].reshape(1, POOL_W),
         "q_a_norm": wts["q_a_norm"].reshape(1, QL), "kv_a_norm": wts["kv_a_norm"].reshape(1, KVL),
         "q_gain": _gain_slab(wts["q_norm_nope"].reshape(1, NOPE), wts["q_norm_rope"].reshape(1, ROPE)),
         "k_gain": _gain_slab(wts["k_norm_nope"].reshape(1, NOPE), wts["k_norm_rope"].reshape(1, ROPE))}
    cos, sin = _rope_tables(positions.reshape(tokens))

    loss8, grad_x, stats, d_pool_grp = _layer_fwd_bwd(x.reshape(tokens, dm), loss_target.reshape(tokens, dm), mod3, cos, sin, plan, P)

    slab = _pack_stats(*stats, loss8, name="pack_stats")
    red, dmod_rows, pool_total = _run(plan, _small_allreduce, slab, d_pool_grp.reshape(POOL_G * LANE, LANE), name="small_allreduce")
    joined = _comm_only(_join_comm([plan.reduced[n] for n in BIG]), name="rs_join_halves")
    grads = {n: g for n, g in zip(BIG, joined)}
    dm_rows = dmod_rows.shape[0] // N_DEV
    dmod_all = dmod_rows.reshape(N_DEV, dm_rows, SLAB_W)[:, :9 * nb].reshape(N_DEV * nb, 9 * dm)
    dmod_cols = lax.dynamic_slice_in_dim(dmod_all, chip * ncol, ncol, axis=1)
    grads["w_ada"] = _ada_bwd(c_all, dmod_cols, name="ada_bwd")

    delta, new_m, new_v = {}, {}, {}
    as2d = lambda a: a.reshape(POOL_G * LANE, LANE) if a.ndim == 4 else a.reshape(1, -1)
    upd, loss_row = _small_update(red, dmod_rows, pool_total, nb,
                                  {n: tuple(as2d(args[p + n]) for p in ("", "m_", "v_")) for n in SMALL}, name="small_update")
    loss = loss_row[0, 0]
    for n in SMALL:
        grads[n], delta[n], new_m[n], new_v[n] = upd[n]
    for n in ("w_ada",) + BIG:
        if n in NARROW:
            res = _adamw(wts[n].T, grads[n].T, mom[n].T, var[n].T, name="adamw_" + n)
            delta[n], new_m[n], new_v[n] = (r.T for r in res)
        else:
            delta[n], new_m[n], new_v[n] = _adamw(wts[n], grads[n], mom[n], var[n], name="adamw_" + n)

    def lead(a, n):
        return a.reshape((1,) + wts[n].shape)

    return (loss, grad_x.reshape(nb, seq, dm), *[lead(grads[n], n) for n in WEIGHTS], *[lead(delta[n], n) for n in WEIGHTS],
            *[lead(new_m[n], n) for n in WEIGHTS], *[lead(new_v[n], n) for n in WEIGHTS])
```

```python
import functools
import math

import jax
import jax.numpy as jnp
from jax import lax
from jax.experimental import pallas as pl
from jax.experimental.pallas import tpu as pltpu

F32 = jnp.float32
MX = jnp.bfloat16

D = 1024
DFF = 2816
NH = 8
POOL_W = 512
POOL_G = 4
QL = 384
KVL = 256
ROPE = 32
NOPE = 64
LANE = 128
SUBLANE = 8
EPS = 1e-6
ATTN_SCALE = 1.0 / math.sqrt(96.0)
NEG = -1e30

Z_UP, Z_QL, Z_KV, Z_KR, Z_GP, Z_GM, Z_W = 0, 512, 896, 1152, 1280, 2304, 3328
KR_LANE = 64
LAT_W = 1280

ADAM_LR, ADAM_B1, ADAM_B2, ADAM_EPS, ADAM_WD, ADAM_STEP = 0.001, 0.9, 0.999, 1e-08, 0.01, 10

VMEM_LIMIT = 48 * 1024 * 1024
MESH = pl.DeviceIdType.MESH
N_DEV = 8
N_CHIP = 4


class _Comm:
    def __init__(self, ins, out_shapes, sems, start, finish, aliases=None):
        self.ins, self.out_shapes, self.sems = list(ins), list(out_shapes), list(sems)
        self.start, self.finish = start, finish
        self.aliases = aliases or {}


def _pcall(body, *, name, out_shape, grid=(), in_specs=None, out_specs=None, scratch=(), grid_spec=None, aliases=None,
           comm=None):
    params = pltpu.CompilerParams(vmem_limit_bytes=VMEM_LIMIT)
    kw = dict(name=name, compiler_params=params)
    if comm is None:
        if aliases:
            kw["input_output_aliases"] = aliases
        if grid_spec is not None:
            return pl.pallas_call(body, grid_spec=grid_spec, out_shape=out_shape, **kw)
        return pl.pallas_call(body, grid=grid, in_specs=in_specs, out_specs=out_specs, scratch_shapes=scratch,
                              out_shape=out_shape, **kw)
    single = not isinstance(out_shape, (list, tuple))
    outs = [out_shape] if single else list(out_shape)
    ospecs = [out_specs] if single else list(out_specs)
    n_in, n_out, n_ci, n_co, n_scr = len(in_specs), len(outs), len(comm.ins), len(comm.out_shapes), len(scratch)
    io = dict(aliases or {})
    io.update({n_in + i: n_out + o for i, o in comm.aliases.items()})

    def riding(*refs):
        ins, cins = refs[:n_in], refs[n_in:n_in + n_ci]
        at = n_in + n_ci
        os_, couts = refs[at:at + n_out], refs[at + n_out:at + n_out + n_co]
        at += n_out + n_co
        scr, csems = refs[at:at + n_scr], refs[at + n_scr:]
        if grid:
            first = functools.reduce(jnp.logical_and, [pl.program_id(d) == 0 for d in range(len(grid))])
            last = functools.reduce(jnp.logical_and, [pl.program_id(d) == grid[d] - 1 for d in range(len(grid))])
            pl.when(first)(lambda: comm.start(cins, couts, csems))
            body(*ins, *os_, *scr)
            pl.when(last)(lambda: comm.finish(cins, couts, csems))
        else:
            comm.start(cins, couts, csems)
            body(*ins, *os_, *scr)
            comm.finish(cins, couts, csems)

    call = pl.pallas_call(riding, grid=grid, in_specs=list(in_specs) + [HBM_SPEC] * n_ci, out_specs=ospecs + [HBM_SPEC] * n_co,
                          out_shape=outs + comm.out_shapes, scratch_shapes=list(scratch) + comm.sems,
                          input_output_aliases=io, **kw)

    def run(*args):
        res = call(*args, *comm.ins)
        main = list(res[:n_out])
        return (main[0] if single else main), list(res[n_out:])

    return run


def _pick(dim, target):
    best = None
    for t in range(LANE, min(dim, target) + 1, LANE):
        if dim % t == 0:
            best = t
    return dim if best is None else best


def _sds(shape, dtype):
    return jax.ShapeDtypeStruct(shape, dtype)


def _dw(a, g, *, name, tn=1024, comm=None):
    return _mm(a, g, mode="tn", out_dtype=F32, name=name, tm=256, tn=tn, tk=a.shape[0], n_outer=True, comm=comm)


def _mm(a, b, *, mode, out_dtype, name, tm=1024, tn=1024, tk=4096, n_outer=False, comm=None):
    if mode == "nn":
        (M, K), (K2, N) = a.shape, b.shape
    elif mode == "nt":
        (M, K), (N, K2) = a.shape, b.shape
    else:
        (K, M), (K2, N) = a.shape, b.shape
    assert K == K2, (name, a.shape, b.shape)
    tm, tn, tk = _pick(M, tm), _pick(N, tn), _pick(K, tk)
    nk = K // tk
    if n_outer:
        ij = lambda g0, g1: (g1, g0)
        grid = (N // tn, M // tm, nk)
    else:
        ij = lambda g0, g1: (g0, g1)
        grid = (M // tm, N // tn, nk)
    if mode == "tn":
        a_spec = pl.BlockSpec((tk, tm), lambda g0, g1, k: (k, ij(g0, g1)[0]))
    else:
        a_spec = pl.BlockSpec((tm, tk), lambda g0, g1, k: (ij(g0, g1)[0], k))
    if mode == "nt":
        b_spec = pl.BlockSpec((tn, tk), lambda g0, g1, k: (ij(g0, g1)[1], k))
    else:
        b_spec = pl.BlockSpec((tk, tn), lambda g0, g1, k: (k, ij(g0, g1)[1]))
    o_spec = pl.BlockSpec((tm, tn), lambda g0, g1, k: ij(g0, g1))
    dn = {"nn": (((1,), (0,)), ((), ())), "nt": (((1,), (1,)), ((), ())), "tn": (((0,), (0,)), ((), ()))}[mode]

    def dot(a_ref, b_ref):
        return lax.dot_general(a_ref[...].astype(MX), b_ref[...].astype(MX), dn, preferred_element_type=F32)

    def body_one(a_ref, b_ref, o_ref):
        o_ref[...] = dot(a_ref, b_ref).astype(o_ref.dtype)

    def body_acc(a_ref, b_ref, o_ref, acc_ref):
        k = pl.program_id(2)
        part = dot(a_ref, b_ref)

        @pl.when(k == 0)
        def _():
            acc_ref[...] = part

        @pl.when(k > 0)
        def _():
            acc_ref[...] += part

        @pl.when(k == nk - 1)
        def _():
            o_ref[...] = acc_ref[...].astype(o_ref.dtype)

    return _pcall(body_one if nk == 1 else body_acc, name=name, out_shape=_sds((M, N), out_dtype), grid=grid,
                  in_specs=[a_spec, b_spec], out_specs=o_spec, scratch=[] if nk == 1 else [pltpu.VMEM((tm, tn), F32)],
                  comm=comm)(a, b)


def _gu_shard(q):
    return (q % 2) * 2 + q // 2


def _ffn_up(h, w_st, *, name, tm=512, comm=None):
    T, dm = h.shape
    hw = w_st.shape[2]
    tm = _pick(T, tm)

    def body(h_ref, wg_ref, wu_ref, gu_ref, a_ref):
        hv = h_ref[...]
        g = jnp.dot(hv, wg_ref[0], preferred_element_type=F32)
        u = jnp.dot(hv, wu_ref[0], preferred_element_type=F32)
        gu_ref[:, :hw] = g.astype(gu_ref.dtype)
        gu_ref[:, hw:] = u.astype(gu_ref.dtype)
        a_ref[...] = (g * _sigmoid(g) * u).astype(a_ref.dtype)

    return _pcall(body, name=name, grid=(T // tm, 2),
                  in_specs=[pl.BlockSpec((tm, dm), lambda i, j: (i, 0)), pl.BlockSpec((1, dm, hw), lambda i, j: (j, 0, 0)),
                            pl.BlockSpec((1, dm, hw), lambda i, j: (2 + j, 0, 0))],
                  out_specs=[pl.BlockSpec((tm, 2 * hw), lambda i, j: (i, j)), pl.BlockSpec((tm, hw), lambda i, j: (i, j))],
                  out_shape=[_sds((T, 4 * hw), MX), _sds((T, 2 * hw), MX)], comm=comm)(h, w_st, w_st)


def _ffn_up_normed(x, mod3, gain, w_st, *, sub, name, tm=512, comm=None):
    T, dm = x.shape
    hw = w_st.shape[2]
    tm = _pick(T, tm)
    tps = (T // mod3.shape[0]) // tm

    def body(x_ref, mod_ref, n_ref, wg_ref, wu_ref, h_ref, gu_ref, a_ref):
        @pl.when(pl.program_id(1) == 0)
        def _():
            xv = x_ref[...]
            xn = xv * _rsq(xv) * n_ref[...]
            h_ref[...] = (xn * (1.0 + mod_ref[0, 3 * sub + 1:3 * sub + 2, :]) + mod_ref[0, 3 * sub:3 * sub + 1, :]).astype(h_ref.dtype)

        hv = h_ref[...]
        g = jnp.dot(hv, wg_ref[0], preferred_element_type=F32)
        u = jnp.dot(hv, wu_ref[0], preferred_element_type=F32)
        gu_ref[:, :hw] = g.astype(gu_ref.dtype)
        gu_ref[:, hw:] = u.astype(gu_ref.dtype)
        a_ref[...] = (g * _sigmoid(g) * u).astype(a_ref.dtype)

    return _pcall(body, name=name, grid=(T // tm, 2),
                  in_specs=[pl.BlockSpec((tm, dm), lambda i, j: (i, 0)), pl.BlockSpec((1, 9, dm), lambda i, j: (i // tps, 0, 0)),
                            pl.BlockSpec((1, dm), lambda i, j: (0, 0)), pl.BlockSpec((1, dm, hw), lambda i, j: (j, 0, 0)),
                            pl.BlockSpec((1, dm, hw), lambda i, j: (2 + j, 0, 0))],
                  out_specs=[pl.BlockSpec((tm, dm), lambda i, j: (i, 0)), pl.BlockSpec((tm, 2 * hw), lambda i, j: (i, j)),
                             pl.BlockSpec((tm, hw), lambda i, j: (i, j))],
                  out_shape=[_sds((T, dm), MX), _sds((T, 4 * hw), MX), _sds((T, 2 * hw), MX)],
                  comm=comm)(x, mod3, gain, w_st, w_st)


def _ffn_dact(df, gu, w_out, *, name, tm=512, comm=None):
    T, dm = df.shape
    hw = gu.shape[1] // 4
    tm = _pick(T, tm)

    def body(df_ref, gu_ref, wo_ref, dgu_ref):
        da = lax.dot_general(df_ref[...], wo_ref[...], (((1,), (1,)), ((), ())), preferred_element_type=F32)
        g = gu_ref[:, :hw].astype(F32)
        u = gu_ref[:, hw:].astype(F32)
        s = _sigmoid(g)
        dgu_ref[:, :hw] = (da * u * (s * (1.0 + g * (1.0 - s)))).astype(dgu_ref.dtype)
        dgu_ref[:, hw:] = (da * (g * s)).astype(dgu_ref.dtype)

    return _pcall(body, name=name, grid=(T // tm, 2),
                  in_specs=[pl.BlockSpec((tm, dm), lambda i, j: (i, 0)), pl.BlockSpec((tm, 2 * hw), lambda i, j: (i, j)),
                            pl.BlockSpec((hw, dm), lambda i, j: (j, 0))],
                  out_specs=pl.BlockSpec((tm, 2 * hw), lambda i, j: (i, j)), out_shape=_sds(gu.shape, MX),
                  comm=comm)(df, gu, w_out)


def _ffn_dh(dgu, w_st, *, name, tm=1024, comm=None):
    T = dgu.shape[0]
    _, dm, hw = w_st.shape
    tm = _pick(T, tm)

    def body(d_ref, w_ref, o_ref, acc_ref):
        q = pl.program_id(1)
        part = lax.dot_general(d_ref[...], w_ref[0], (((1,), (1,)), ((), ())), preferred_element_type=F32)

        @pl.when(q == 0)
        def _():
            acc_ref[...] = part

        @pl.when(q > 0)
        def _():
            acc_ref[...] += part

        @pl.when(q == 3)
        def _():
            o_ref[...] = acc_ref[...]

    return _pcall(body, name=name, grid=(T // tm, 4),
                  in_specs=[pl.BlockSpec((tm, hw), lambda i, q: (i, q)), pl.BlockSpec((1, dm, hw), lambda i, q: (_gu_shard(q), 0, 0))],
                  out_specs=pl.BlockSpec((tm, dm), lambda i, q: (i, 0)), out_shape=_sds((T, dm), F32),
                  scratch=[pltpu.VMEM((tm, dm), F32)], comm=comm)(dgu, w_st)


def _ffn_dw_in(h, dgu, *, name, tm=256, comm=None):
    T, dm = h.shape
    hw = dgu.shape[1] // 4
    tm = _pick(dm, tm)

    def body(h_ref, d_ref, o_ref):
        o_ref[0] = lax.dot_general(h_ref[...], d_ref[...], (((0,), (0,)), ((), ())), preferred_element_type=F32)

    return _pcall(body, name=name, grid=(4, dm // tm),
                  in_specs=[pl.BlockSpec((T, tm), lambda q, i: (0, i)), pl.BlockSpec((T, hw), lambda q, i: (0, q))],
                  out_specs=pl.BlockSpec((1, tm, hw), lambda q, i: (_gu_shard(q), i, 0)),
                  out_shape=_sds((4, dm, hw), F32), comm=comm)(h, dgu)


def _rsq(x):
    return lax.rsqrt(jnp.mean(x * x, axis=-1, keepdims=True) + EPS)


def _sigmoid(x):
    return 1.0 / (1.0 + jnp.exp(-x))


def _row_spec(tm, w):
    return pl.BlockSpec((tm, w), lambda i: (i, 0))


def _fwd_block(x_prev, f_prev, mod3, gain, *, sub, coef, name, tm=256):
    T, dm = x_prev.shape
    tps = (T // mod3.shape[0]) // tm
    has_f = f_prev is not None
    mod_spec = pl.BlockSpec((1, 9, dm), lambda i: (i // tps, 0, 0))
    vec_spec = pl.BlockSpec((1, dm), lambda i: (0, 0))

    def body(*refs):
        if has_f:
            x_ref, f_ref, mod_ref, n_ref, xo_ref, h_ref = refs
            x = x_ref[...] + coef * mod_ref[0, 3 * sub - 1:3 * sub, :] * f_ref[...]
            xo_ref[...] = x
        else:
            x_ref, mod_ref, n_ref, h_ref = refs
            x = x_ref[...]
        xn = x * _rsq(x) * n_ref[...]
        h = xn * (1.0 + mod_ref[0, 3 * sub + 1:3 * sub + 2, :]) + mod_ref[0, 3 * sub:3 * sub + 1, :]
        h_ref[...] = h.astype(h_ref.dtype)

    row = _row_spec(tm, dm)
    if has_f:
        return _pcall(body, name=name, grid=(T // tm,), in_specs=[row, row, mod_spec, vec_spec], out_specs=[row, row],
                      out_shape=[_sds((T, dm), F32), _sds((T, dm), MX)])(x_prev, f_prev, mod3, gain)
    return _pcall(body, name=name, grid=(T // tm,), in_specs=[row, mod_spec, vec_spec], out_specs=row,
                  out_shape=_sds((T, dm), MX))(x_prev, mod3, gain)


def _mm_resid_norm(a, w, x_prev, mod3, gain, *, sub, coef, name, tm=512, comm=None):
    T, k = a.shape
    dm = w.shape[1]
    tm = _pick(T, tm)
    tps = (T // mod3.shape[0]) // tm

    def body(a_ref, w_ref, x_ref, mod_ref, n_ref, f_ref, xo_ref, h_ref):
        f = jnp.dot(a_ref[...], w_ref[...], preferred_element_type=F32)
        f_ref[...] = f
        x = x_ref[...] + coef * mod_ref[0, 3 * sub - 1:3 * sub, :] * f
        xo_ref[...] = x
        xn = x * _rsq(x) * n_ref[...]
        h_ref[...] = (xn * (1.0 + mod_ref[0, 3 * sub + 1:3 * sub + 2, :]) + mod_ref[0, 3 * sub:3 * sub + 1, :]).astype(h_ref.dtype)

    row = _row_spec(tm, dm)
    return _pcall(body, name=name, grid=(T // tm,),
                  in_specs=[_row_spec(tm, k), pl.BlockSpec((k, dm), lambda i: (0, 0)), row,
                            pl.BlockSpec((1, 9, dm), lambda i: (i // tps, 0, 0)), pl.BlockSpec((1, dm), lambda i: (0, 0))],
                  out_specs=[row, row, row], out_shape=[_sds((T, dm), F32), _sds((T, dm), F32), _sds((T, dm), MX)],
                  comm=comm)(a, w, x_prev, mod3, gain)


def _mm_loss(a, w, x2, tgt, mod3, *, name, tm=512):
    T, k = a.shape
    dm = w.shape[1]
    tm = _pick(T, tm)
    tps = (T // mod3.shape[0]) // tm
    mod_spec = pl.BlockSpec((1, 9, dm), lambda i: (i // tps, 0, 0))
    row = _row_spec(tm, dm)
    stat_spec = pl.BlockSpec((1, SUBLANE, dm), lambda i: (i // tps, 0, 0))
    loss_spec = pl.BlockSpec((SUBLANE, LANE), lambda i: (0, 0))

    def body(a_ref, w_ref, x_ref, t_ref, mod_ref, dy_ref, df_ref, st_ref, loss_ref):
        i = pl.program_id(0)
        g = mod_ref[0, 8:9, :]
        f = jnp.dot(a_ref[...], w_ref[...], preferred_element_type=F32)
        err = x_ref[...] + 0.5 * g * f - t_ref[...]
        dy = err * (1.0 / dm)
        dy_ref[...] = dy
        df_ref[...] = (0.5 * g * dy).astype(df_ref.dtype)
        dgate = jnp.sum(0.5 * dy * f, axis=0, keepdims=True)
        part = 0.5 * jnp.sum(jnp.sum(err * err, axis=0, keepdims=True), axis=1, keepdims=True) * (1.0 / dm)

        @pl.when(i % tps == 0)
        def _():
            st_ref[...] = jnp.zeros_like(st_ref)

        @pl.when(i == 0)
        def _():
            loss_ref[...] = jnp.zeros_like(loss_ref)

        st_ref[0, 0:1, :] += dgate
        loss_ref[...] += jnp.broadcast_to(part, loss_ref.shape)

    return _pcall(body, name=name, grid=(T // tm,),
                  in_specs=[_row_spec(tm, k), pl.BlockSpec((k, dm), lambda i: (0, 0)), row, row, mod_spec],
                  out_specs=[row, row, stat_spec, loss_spec],
                  out_shape=[_sds((T, dm), F32), _sds((T, dm), MX), _sds((mod3.shape[0], SUBLANE, dm), F32),
                             _sds((SUBLANE, LANE), F32)])(a, w, x2, tgt, mod3)


def _norm_bwd_tail(dhv, x_ref, dxi_ref, f_ref, mod_ref, n_ref, dx_ref, df_ref, st_ref, *, first, sub, coef):
    x = x_ref[...]
    r = _rsq(x)
    xhat = x * r
    n = n_ref[...]
    d_shift = jnp.sum(dhv, axis=0, keepdims=True)
    d_scale = jnp.sum(dhv * (xhat * n), axis=0, keepdims=True)
    dxn = dhv * (1.0 + mod_ref[0, 3 * sub + 1:3 * sub + 2, :])
    d_gain = jnp.sum(dxn * xhat, axis=0, keepdims=True)
    dxhat = dxn * n
    dx = dxi_ref[...] + r * (dxhat - xhat * jnp.mean(dxhat * xhat, axis=-1, keepdims=True))
    dx_ref[...] = dx

    @pl.when(first)
    def _():
        st_ref[...] = jnp.zeros_like(st_ref)

    st_ref[0, 0:1, :] += d_shift
    st_ref[0, 1:2, :] += d_scale
    st_ref[0, 2:3, :] += d_gain
    if f_ref is not None:
        st_ref[0, 3:4, :] += jnp.sum(coef * dx * f_ref[...], axis=0, keepdims=True)
        df_ref[...] = (coef * mod_ref[0, 3 * sub - 1:3 * sub, :] * dx).astype(df_ref.dtype)


def _ffn_dh_norm(dgu, w_st, x_cur, dx_in, f_prev, mod3, gain, *, sub, coef, name, tm=512, comm=None):
    T = dgu.shape[0]
    _, dm, hw = w_st.shape
    tm = _pick(T, tm)
    nb = mod3.shape[0]
    tps = (T // nb) // tm

    def body(d_ref, w_ref, x_ref, dxi_ref, f_ref, mod_ref, n_ref, dx_ref, df_ref, st_ref, acc_ref):
        i = pl.program_id(0)
        q = pl.program_id(1)
        part = lax.dot_general(d_ref[...], w_ref[0], (((1,), (1,)), ((), ())), preferred_element_type=F32)

        @pl.when(q == 0)
        def _():
            acc_ref[...] = part

        @pl.when(q > 0)
        def _():
            acc_ref[...] += part

        @pl.when(q == 3)
        def _():
            _norm_bwd_tail(acc_ref[...], x_ref, dxi_ref, f_ref, mod_ref, n_ref, dx_ref, df_ref, st_ref,
                           first=i % tps == 0, sub=sub, coef=coef)

    row = pl.BlockSpec((tm, dm), lambda i, q: (i, 0))
    return _pcall(body, name=name, grid=(T // tm, 4),
                  in_specs=[pl.BlockSpec((tm, hw), lambda i, q: (i, q)), pl.BlockSpec((1, dm, hw), lambda i, q: (_gu_shard(q), 0, 0)),
                            row, row, row, pl.BlockSpec((1, 9, dm), lambda i, q: (i // tps, 0, 0)),
                            pl.BlockSpec((1, dm), lambda i, q: (0, 0))],
                  out_specs=[row, row, pl.BlockSpec((1, SUBLANE, dm), lambda i, q: (i // tps, 0, 0))],
                  out_shape=[_sds((T, dm), F32), _sds((T, dm), MX), _sds((nb, SUBLANE, dm), F32)],
                  scratch=[pltpu.VMEM((tm, dm), F32)], comm=comm)(dgu, w_st, x_cur, dx_in, f_prev, mod3, gain)


def _mm_nt_norm(a, b, x_cur, dx_in, f_prev, mod3, gain, *, sub, coef, name, tm=512, comm=None):
    T, k = a.shape
    dm = b.shape[0]
    tm = _pick(T, tm)
    nb = mod3.shape[0]
    tps = (T // nb) // tm

    def body(a_ref, b_ref, x_ref, dxi_ref, f_ref, mod_ref, n_ref, dx_ref, df_ref, st_ref):
        dh = lax.dot_general(a_ref[...], b_ref[...], (((1,), (1,)), ((), ())), preferred_element_type=F32)
        _norm_bwd_tail(dh, x_ref, dxi_ref, f_ref, mod_ref, n_ref, dx_ref, df_ref, st_ref,
                       first=pl.program_id(0) % tps == 0, sub=sub, coef=coef)

    row = _row_spec(tm, dm)
    return _pcall(body, name=name, grid=(T // tm,),
                  in_specs=[_row_spec(tm, k), pl.BlockSpec((dm, k), lambda i: (0, 0)), row, row, row,
                            pl.BlockSpec((1, 9, dm), lambda i: (i // tps, 0, 0)), pl.BlockSpec((1, dm), lambda i: (0, 0))],
                  out_specs=[row, row, pl.BlockSpec((1, SUBLANE, dm), lambda i: (i // tps, 0, 0))],
                  out_shape=[_sds((T, dm), F32), _sds((T, dm), MX), _sds((nb, SUBLANE, dm), F32)],
                  comm=comm)(a, b, x_cur, dx_in, f_prev, mod3, gain)


def _bwd_block(x_cur, dh, dx_in, f_prev, mod3, gain, *, sub, coef, name, tm=256, comm=None):
    T, dm = x_cur.shape
    nb = mod3.shape[0]
    tps = (T // nb) // tm
    has_f = f_prev is not None
    mod_spec = pl.BlockSpec((1, 9, dm), lambda i: (i // tps, 0, 0))
    vec_spec = pl.BlockSpec((1, dm), lambda i: (0, 0))
    stat_spec = pl.BlockSpec((1, SUBLANE, dm), lambda i: (i // tps, 0, 0))
    row = _row_spec(tm, dm)

    def body(*refs):
        if has_f:
            x_ref, dh_ref, dxi_ref, f_ref, mod_ref, n_ref, dx_ref, df_ref, st_ref = refs
        else:
            x_ref, dh_ref, dxi_ref, mod_ref, n_ref, dx_ref, st_ref = refs
            f_ref = df_ref = None
        _norm_bwd_tail(dh_ref[...], x_ref, dxi_ref, f_ref, mod_ref, n_ref, dx_ref, df_ref, st_ref,
                       first=pl.program_id(0) % tps == 0, sub=sub, coef=coef)

    st_shape = _sds((nb, SUBLANE, dm), F32)
    if has_f:
        return _pcall(body, name=name, grid=(T // tm,), in_specs=[row, row, row, row, mod_spec, vec_spec],
                      out_specs=[row, row, stat_spec],
                      out_shape=[_sds((T, dm), F32), _sds((T, dm), MX), st_shape], comm=comm)(x_cur, dh, dx_in, f_prev, mod3, gain)
    return _pcall(body, name=name, grid=(T // tm,), in_specs=[row, row, row, mod_spec, vec_spec],
                  out_specs=[row, stat_spec], out_shape=[_sds((T, dm), F32), st_shape], comm=comm)(x_cur, dh, dx_in, mod3, gain)


def _merge_fwd(z, br_pool, br_mla, *, name, tm=256):
    T = z.shape[0]

    def body(z_ref, bp_ref, bm_ref, o_ref):
        gp = z_ref[:, Z_GP:Z_GP + D]
        gm = z_ref[:, Z_GM:Z_GM + D]
        o_ref[...] = (_sigmoid(gp) * bp_ref[...] + _sigmoid(gm) * bm_ref[...]).astype(o_ref.dtype)

    return _pcall(body, name=name, grid=(T // tm,), in_specs=[_row_spec(tm, Z_W), _row_spec(tm, D), _row_spec(tm, D)],
                  out_specs=_row_spec(tm, D), out_shape=_sds((T, D), MX))(z, br_pool, br_mla)


def _merge_bwd(z, br_pool, br_mla, dmerged, *, name, tm=256):
    T = z.shape[0]

    def body(z_ref, bp_ref, bm_ref, dm_ref, dbp_ref, dbm_ref, dg_ref):
        dm = dm_ref[...]
        sp = _sigmoid(z_ref[:, Z_GP:Z_GP + D])
        sm = _sigmoid(z_ref[:, Z_GM:Z_GM + D])
        dbp_ref[...] = (dm * sp).astype(dbp_ref.dtype)
        dbm_ref[...] = (dm * sm).astype(dbm_ref.dtype)
        dg_ref[:, :D] = (dm * bp_ref[...] * sp * (1.0 - sp)).astype(dg_ref.dtype)
        dg_ref[:, D:] = (dm * bm_ref[...] * sm * (1.0 - sm)).astype(dg_ref.dtype)

    return _pcall(body, name=name, grid=(T // tm,),
                  in_specs=[_row_spec(tm, Z_W), _row_spec(tm, D), _row_spec(tm, D), _row_spec(tm, D)],
                  out_specs=[_row_spec(tm, D), _row_spec(tm, D), _row_spec(tm, 2 * D)],
                  out_shape=[_sds((T, D), MX), _sds((T, D), MX), _sds((T, 2 * D), MX)])(z, br_pool, br_mla, dmerged)


def _shift_down(x, k, row):
    return jnp.where(row >= k, pltpu.roll(x, k, 0), 0.0)


def _shift_up(x, k, row, n):
    return jnp.where(row < n - k, pltpu.roll(x, n - k, 0), 0.0)


def _pool_fwd(z, pool_grp, pool_scale, *, nb, name):
    T = z.shape[0]
    S = T // nb
    blk = pl.BlockSpec((S, LANE), lambda b, g: (b, g))

    def body(u_ref, w_ref, s_ref, pooled_ref, mixed_ref, scaled_ref):
        g = pl.program_id(1)
        u = u_ref[...]
        row = lax.broadcasted_iota(jnp.int32, u.shape, 0)
        s2 = u + _shift_down(u, 1, row)
        s4 = s2 + _shift_down(s2, 2, row)
        s8 = s4 + _shift_down(s4, 4, row)
        s16 = s8 + _shift_down(s8, 8, row)
        win = jnp.where(g == 0, s2, jnp.where(g == 1, s4, jnp.where(g == 2, s8, s16)))
        width = lax.shift_left(jnp.int32(2), g)
        cnt = jnp.minimum(row + 1, width).astype(F32)
        pooled = (win / cnt - u).astype(MX)
        pooled_ref[...] = pooled
        mixed = jnp.dot(pooled, w_ref[0], preferred_element_type=F32)
        mixed_ref[...] = mixed
        scaled_ref[...] = (mixed * s_ref[...]).astype(scaled_ref.dtype)

    return _pcall(body, name=name, grid=(nb, POOL_G),
                  in_specs=[blk, pl.BlockSpec((1, LANE, LANE), lambda b, g: (g, 0, 0)),
                            pl.BlockSpec((1, LANE), lambda b, g: (0, g))],
                  out_specs=[blk, blk, blk],
                  out_shape=[_sds((T, POOL_W), MX), _sds((T, POOL_W), F32), _sds((T, POOL_W), MX)])(z, pool_grp, pool_scale)


def _pool_bwd(dscaled, mixed, pooled, pool_grp, pool_scale, *, nb, name):
    T = dscaled.shape[0]
    S = T // nb
    blk = pl.BlockSpec((S, LANE), lambda g, b: (b, g))

    def body(ds_ref, mixed_ref, pooled_ref, w_ref, s_ref, du_ref, dw_ref, dsc_ref):
        g = pl.program_id(0)
        b = pl.program_id(1)
        ds = ds_ref[...]
        dsc_ref[0] = jnp.sum(ds * mixed_ref[...], axis=0, keepdims=True)
        dmixed = (ds * s_ref[...]).astype(MX)
        dw = lax.dot_general(pooled_ref[...], dmixed, (((0,), (0,)), ((), ())), preferred_element_type=F32)

        @pl.when(b == 0)
        def _():
            dw_ref[0] = dw

        @pl.when(b > 0)
        def _():
            dw_ref[0] += dw
        dpooled = lax.dot_general(dmixed, w_ref[0], (((1,), (1,)), ((), ())), preferred_element_type=F32)
        row = lax.broadcasted_iota(jnp.int32, dpooled.shape, 0)
        width = lax.shift_left(jnp.int32(2), g)
        q = dpooled / jnp.minimum(row + 1, width).astype(F32)
        r2 = q + _shift_up(q, 1, row, S)
        r4 = r2 + _shift_up(r2, 2, row, S)
        r8 = r4 + _shift_up(r4, 4, row, S)
        r16 = r8 + _shift_up(r8, 8, row, S)
        win = jnp.where(g == 0, r2, jnp.where(g == 1, r4, jnp.where(g == 2, r8, r16)))
        du_ref[...] = (win - dpooled).astype(du_ref.dtype)

    return _pcall(body, name=name, grid=(POOL_G, nb),
                  in_specs=[blk, blk, blk, pl.BlockSpec((1, LANE, LANE), lambda g, b: (g, 0, 0)),
                            pl.BlockSpec((1, LANE), lambda g, b: (0, g))],
                  out_specs=[blk, pl.BlockSpec((1, LANE, LANE), lambda g, b: (g, 0, 0)),
                             pl.BlockSpec((1, 1, LANE), lambda g, b: (b, 0, g))],
                  out_shape=[_sds((T, POOL_W), MX), _sds((POOL_G, LANE, LANE), F32), _sds((nb, 1, POOL_W), F32)],
                  )(dscaled, mixed, pooled, pool_grp, pool_scale)


def _lat_fwd(z, q_gain, kv_gain, *, name, tm=256):
    T = z.shape[0]

    def body(z_ref, qg_ref, kg_ref, qn_ref, kvn_ref):
        ql = z_ref[:, Z_QL:Z_QL + QL]
        kv = z_ref[:, Z_KV:Z_KV + KVL]
        qn_ref[...] = (ql * _rsq(ql) * qg_ref[...]).astype(qn_ref.dtype)
        kvn_ref[...] = (kv * _rsq(kv) * kg_ref[...]).astype(kvn_ref.dtype)

    return _pcall(body, name=name, grid=(T // tm,),
                  in_specs=[_row_spec(tm, LAT_W), pl.BlockSpec((1, QL), lambda i: (0, 0)), pl.BlockSpec((1, KVL), lambda i: (0, 0))],
                  out_specs=[_row_spec(tm, QL), _row_spec(tm, KVL)],
                  out_shape=[_sds((T, QL), MX), _sds((T, KVL), MX)])(z, q_gain, kv_gain)


def _lat_bwd(z, dqn, dkvn, q_gain, kv_gain, *, nb, name, tm=256):
    T = z.shape[0]
    tps = (T // nb) // tm

    def norm_bwd(x, dy, gain):
        r = _rsq(x)
        xhat = x * r
        dgain = jnp.sum(dy * xhat, axis=0, keepdims=True)
        dxhat = dy * gain
        return r * (dxhat - xhat * jnp.mean(dxhat * xhat, axis=-1, keepdims=True)), dgain

    def body(z_ref, dq_ref, dkv_ref, qg_ref, kg_ref, dql_ref, dkvl_ref, sq_ref, sk_ref):
        i = pl.program_id(0)
        dql, dqg = norm_bwd(z_ref[:, Z_QL:Z_QL + QL], dq_ref[...], qg_ref[...])
        dkvl, dkg = norm_bwd(z_ref[:, Z_KV:Z_KV + KVL], dkv_ref[...], kg_ref[...])
        dql_ref[...] = dql.astype(dql_ref.dtype)
        dkvl_ref[...] = dkvl.astype(dkvl_ref.dtype)

        @pl.when(i % tps == 0)
        def _():
            sq_ref[...] = jnp.zeros_like(sq_ref)
            sk_ref[...] = jnp.zeros_like(sk_ref)

        sq_ref[0, 0:1, :] += dqg
        sk_ref[0, 0:1, :] += dkg

    return _pcall(body, name=name, grid=(T // tm,),
                  in_specs=[_row_spec(tm, LAT_W), _row_spec(tm, QL), _row_spec(tm, KVL),
                            pl.BlockSpec((1, QL), lambda i: (0, 0)), pl.BlockSpec((1, KVL), lambda i: (0, 0))],
                  out_specs=[_row_spec(tm, QL), _row_spec(tm, KVL),
                             pl.BlockSpec((1, SUBLANE, QL), lambda i: (i // tps, 0, 0)),
                             pl.BlockSpec((1, SUBLANE, KVL), lambda i: (i // tps, 0, 0))],
                  out_shape=[_sds((T, QL), MX), _sds((T, KVL), MX), _sds((nb, SUBLANE, QL), F32), _sds((nb, SUBLANE, KVL), F32)],
                  )(z, dqn, dkvn, q_gain, kv_gain)


def _lane_masks(shape):
    lane = lax.broadcasted_iota(jnp.int32, shape, len(shape) - 1)
    m_n = lane < NOPE
    m_r = jnp.logical_and(lane >= KR_LANE, lane < KR_LANE + ROPE)
    first_half = lane < KR_LANE + ROPE // 2
    return m_n, m_r, first_half


def _rot(y, first_half):
    return jnp.where(first_half, -pltpu.roll(y, LANE - ROPE // 2, 1), pltpu.roll(y, ROPE // 2, 1))


def _rot_t(v, first_half, m_r):
    return jnp.where(m_r, jnp.where(first_half, pltpu.roll(v, LANE - ROPE // 2, 1), -pltpu.roll(v, ROPE // 2, 1)), 0.0)


def _prep_fwd(qp, kvp, z, cos, sin, q_gain, k_gain, *, name, tm=256):
    T = qp.shape[0]
    slab = pl.BlockSpec((tm, LANE), lambda i: (i, 0))
    kr_spec = pl.BlockSpec((tm, LANE), lambda i: (i, Z_KR // LANE))
    vec = pl.BlockSpec((1, LANE), lambda i: (0, 0))

    def body(qp_ref, kvp_ref, kr_ref, cos_ref, sin_ref, qg_ref, kg_ref, q_ref, k_ref, v_ref):
        m_n, m_r, first_half = _lane_masks((tm, LANE))
        c = cos_ref[...]
        s = sin_ref[...]
        qg = qg_ref[...]
        kg = kg_ref[...]
        xr = kr_ref[...]
        rr = lax.rsqrt(jnp.sum(xr * xr, axis=-1, keepdims=True) * (1.0 / ROPE) + EPS)
        yr = xr * rr * kg
        kr = jnp.where(m_r, yr * c + _rot(yr, first_half) * s, 0.0)
        for h in range(NH):
            x = qp_ref[:, h * LANE:(h + 1) * LANE]
            x2 = x * x
            rn = lax.rsqrt(jnp.sum(jnp.where(m_n, x2, 0.0), axis=-1, keepdims=True) * (1.0 / NOPE) + EPS)
            rq = lax.rsqrt(jnp.sum(jnp.where(m_r, x2, 0.0), axis=-1, keepdims=True) * (1.0 / ROPE) + EPS)
            y = x * jnp.where(m_n, rn, jnp.where(m_r, rq, 0.0)) * qg
            q_ref[:, h * LANE:(h + 1) * LANE] = (y * c + _rot(y, first_half) * s).astype(q_ref.dtype)
            xk = kvp_ref[:, h * LANE:(h + 1) * LANE]
            rk = lax.rsqrt(jnp.sum(jnp.where(m_n, xk * xk, 0.0), axis=-1, keepdims=True) * (1.0 / NOPE) + EPS)
            k_ref[:, h * LANE:(h + 1) * LANE] = (jnp.where(m_n, xk * rk * kg, 0.0) + kr).astype(k_ref.dtype)
        v_ref[...] = kvp_ref[:, NH * LANE:].astype(v_ref.dtype)

    return _pcall(body, name=name, grid=(T // tm,),
                  in_specs=[_row_spec(tm, NH * LANE), _row_spec(tm, NH * LANE + NH * NOPE), kr_spec, slab, slab, vec, vec],
                  out_specs=[_row_spec(tm, NH * LANE), _row_spec(tm, NH * LANE), _row_spec(tm, NH * NOPE)],
                  out_shape=[_sds((T, NH * LANE), MX), _sds((T, NH * LANE), MX), _sds((T, NH * NOPE), MX)],
                  )(qp, kvp, z, cos, sin, q_gain, k_gain)


def _prep_bwd(dq, dk, dv, qp, kvp, z, cos, sin, q_gain, k_gain, *, nb, name, tm=256):
    T = qp.shape[0]
    tps = (T // nb) // tm
    slab = pl.BlockSpec((tm, LANE), lambda i: (i, 0))
    kr_spec = pl.BlockSpec((tm, LANE), lambda i: (i, Z_KR // LANE))
    vec = pl.BlockSpec((1, LANE), lambda i: (0, 0))

    def body(dq_ref, dk_ref, dv_ref, qp_ref, kvp_ref, kr_ref, cos_ref, sin_ref, qg_ref, kg_ref,
             dqp_ref, dkvp_ref, dkr_ref, st_ref):
        i = pl.program_id(0)
        m_n, m_r, first_half = _lane_masks((tm, LANE))
        c = cos_ref[...]
        s = sin_ref[...]
        qg = qg_ref[...]
        kg = kg_ref[...]
        dqg = jnp.zeros((1, LANE), F32)
        dkg = jnp.zeros((1, LANE), F32)
        dkr_sum = jnp.zeros((tm, LANE), F32)
        for h in range(NH):
            x = qp_ref[:, h * LANE:(h + 1) * LANE]
            x2 = x * x
            rn = lax.rsqrt(jnp.sum(jnp.where(m_n, x2, 0.0), axis=-1, keepdims=True) * (1.0 / NOPE) + EPS)
            rq = lax.rsqrt(jnp.sum(jnp.where(m_r, x2, 0.0), axis=-1, keepdims=True) * (1.0 / ROPE) + EPS)
            rfac = jnp.where(m_n, rn, jnp.where(m_r, rq, 0.0))
            xhat = x * rfac
            do = dq_ref[:, h * LANE:(h + 1) * LANE]
            dy = do * c + _rot_t(do * s, first_half, m_r)
            dqg = dqg + jnp.sum(dy * xhat, axis=0, keepdims=True)
            dxhat = dy * qg
            t = dxhat * xhat
            mean_n = jnp.sum(jnp.where(m_n, t, 0.0), axis=-1, keepdims=True) * (1.0 / NOPE)
            mean_r = jnp.sum(jnp.where(m_r, t, 0.0), axis=-1, keepdims=True) * (1.0 / ROPE)
            dqp_ref[:, h * LANE:(h + 1) * LANE] = (
                rfac * (dxhat - xhat * jnp.where(m_n, mean_n, jnp.where(m_r, mean_r, 0.0)))).astype(dqp_ref.dtype)

            xk = kvp_ref[:, h * LANE:(h + 1) * LANE]
            rk = lax.rsqrt(jnp.sum(jnp.where(m_n, xk * xk, 0.0), axis=-1, keepdims=True) * (1.0 / NOPE) + EPS)
            khat = jnp.where(m_n, xk * rk, 0.0)
            dko = dk_ref[:, h * LANE:(h + 1) * LANE]
            dkn = jnp.where(m_n, dko, 0.0)
            dkg = dkg + jnp.sum(dkn * khat, axis=0, keepdims=True)
            dkhat = dkn * kg
            mean_k = jnp.sum(dkhat * khat, axis=-1, keepdims=True) * (1.0 / NOPE)
            dkvp_ref[:, h * LANE:(h + 1) * LANE] = jnp.where(m_n, rk * (dkhat - khat * mean_k), 0.0).astype(dkvp_ref.dtype)
            dkr_sum = dkr_sum + jnp.where(m_r, dko, 0.0)
        dkvp_ref[:, NH * LANE:] = dv_ref[...].astype(dkvp_ref.dtype)

        xr = kr_ref[...]
        rr = lax.rsqrt(jnp.sum(xr * xr, axis=-1, keepdims=True) * (1.0 / ROPE) + EPS)
        rhat = xr * rr
        dyr = dkr_sum * c + _rot_t(dkr_sum * s, first_half, m_r)
        dkg = dkg + jnp.sum(dyr * rhat, axis=0, keepdims=True)
        drhat = dyr * kg
        mean_kr = jnp.sum(drhat * rhat, axis=-1, keepdims=True) * (1.0 / ROPE)
        dkr_ref[...] = jnp.where(m_r, rr * (drhat - rhat * mean_kr), 0.0).astype(dkr_ref.dtype)

        @pl.when(i % tps == 0)
        def _():
            st_ref[...] = jnp.zeros_like(st_ref)

        st_ref[0, 0:1, :] += dqg
        st_ref[0, 1:2, :] += dkg

    return _pcall(body, name=name, grid=(T // tm,),
                  in_specs=[_row_spec(tm, NH * LANE), _row_spec(tm, NH * LANE), _row_spec(tm, NH * NOPE),
                            _row_spec(tm, NH * LANE), _row_spec(tm, NH * LANE + NH * NOPE), kr_spec, slab, slab, vec, vec],
                  out_specs=[_row_spec(tm, NH * LANE), _row_spec(tm, NH * LANE + NH * NOPE), slab,
                             pl.BlockSpec((1, SUBLANE, LANE), lambda i: (i // tps, 0, 0))],
                  out_shape=[_sds((T, NH * LANE), MX), _sds((T, NH * LANE + NH * NOPE), MX), _sds((T, LANE), MX),
                             _sds((nb, SUBLANE, LANE), F32)],
                  )(dq, dk, dv, qp, kvp, z, cos, sin, q_gain, k_gain)


def _lower_triangle(t):
    return lax.broadcasted_iota(jnp.int32, (t, t), 1) <= lax.broadcasted_iota(jnp.int32, (t, t), 0)


def _attn_fwd(q, k, v, *, nb, name, tq=512, comm=None):
    T = q.shape[0]
    S = T // nb
    tq = _pick(S, tq)
    nq = S // tq
    tk = tq
    npair = NH // 2

    def body(q_ref, k_ref, v_ref, o_ref, lse_ref):
        qi = pl.program_id(2)
        lane = lax.broadcasted_iota(jnp.int32, (tq, LANE), 1)
        qs = [q_ref[:, hh * LANE:(hh + 1) * LANE] for hh in range(2)]

        def block(j, carry, diagonal):
            k0 = pl.multiple_of(j * tk, tk)
            vb = v_ref[pl.ds(k0, tk), :]
            new = []
            for hh in range(2):
                m, l, acc = carry[hh]
                kb = k_ref[pl.ds(k0, tk), hh * LANE:(hh + 1) * LANE]
                s = lax.dot_general(qs[hh], kb, (((1,), (1,)), ((), ())), preferred_element_type=F32) * ATTN_SCALE
                if diagonal:
                    s = jnp.where(_lower_triangle(tq), s, NEG)
                m_new = jnp.maximum(m, jnp.max(s, axis=-1, keepdims=True))
                p = jnp.exp(s - m_new)
                alpha = jnp.exp(m - m_new)
                l = alpha * l + jnp.sum(p, axis=-1, keepdims=True)
                acc = alpha * acc + jnp.dot(p.astype(MX), vb, preferred_element_type=F32)
                new.append((m_new, l, acc))
            return tuple(new)

        init = tuple((jnp.full((tq, 1), NEG, F32), jnp.zeros((tq, 1), F32), jnp.zeros((tq, LANE), F32)) for _ in range(2))
        carry = lax.fori_loop(0, qi, lambda j, c: block(j, c, False), init)
        (m0, l0, acc0), (m1, l1, acc1) = block(qi, carry, True)
        o_ref[...] = jnp.where(lane < NOPE, acc0 / l0, acc1 / l1).astype(o_ref.dtype)
        lse_ref[...] = jnp.where(lane < NOPE, m0 + jnp.log(l0), m1 + jnp.log(l1))

    return _pcall(body, name=name, grid=(nb, npair, nq),
                  in_specs=[pl.BlockSpec((tq, 2 * LANE), lambda b, p, i: (b * nq + i, p)),
                            pl.BlockSpec((S, 2 * LANE), lambda b, p, i: (b, p)),
                            pl.BlockSpec((S, LANE), lambda b, p, i: (b, p))],
                  out_specs=[pl.BlockSpec((tq, LANE), lambda b, p, i: (b * nq + i, p)),
                             pl.BlockSpec((tq, LANE), lambda b, p, i: (b * nq + i, p))],
                  out_shape=[_sds((T, NH * NOPE), MX), _sds((T, NH * NOPE), F32)], comm=comm)(q, k, v)


def _attn_bwd(q, k, v, o, lse, do, *, nb, name, tq=512, comm=None):
    T = q.shape[0]
    S = T // nb
    tq = _pick(S, tq)
    nq = S // tq
    tk = tq
    npair = NH // 2

    def body(q_ref, k_ref, v_ref, o_ref, lse_ref, do_ref, dq_ref, dk_ref, dv_ref, delta_ref):
        lane = lax.broadcasted_iota(jnp.int32, (tq, LANE), 1)
        first = lane < NOPE
        dq_ref[...] = jnp.zeros_like(dq_ref)

        def delta_step(qi, _):
            q0 = pl.multiple_of(qi * tq, tq)
            prod = do_ref[pl.ds(q0, tq), :] * o_ref[pl.ds(q0, tq), :].astype(F32)
            d0 = jnp.sum(jnp.where(first, prod, 0.0), axis=-1, keepdims=True)
            d1 = jnp.sum(jnp.where(first, 0.0, prod), axis=-1, keepdims=True)
            delta_ref[pl.ds(q0, tq), :] = jnp.where(first, d0, d1)
            return 0

        lax.fori_loop(0, nq, delta_step, 0)

        def kv_step(kj, _):
            k0 = pl.multiple_of(kj * tk, tk)
            kbs = [k_ref[pl.ds(k0, tk), hh * LANE:(hh + 1) * LANE] for hh in range(2)]
            vb = v_ref[pl.ds(k0, tk), :]

            def q_block(qi, carry, diagonal):
                dk0, dk1, dv = carry
                dks = [dk0, dk1]
                q0 = pl.multiple_of(qi * tq, tq)
                dov = do_ref[pl.ds(q0, tq), :]
                lse_v = lse_ref[pl.ds(q0, tq), :]
                delta_v = delta_ref[pl.ds(q0, tq), :]
                for hh in range(2):
                    qh = q_ref[pl.ds(q0, tq), hh * LANE:(hh + 1) * LANE]
                    dob = jnp.where(first if hh == 0 else jnp.logical_not(first), dov, 0.0).astype(MX)
                    s = lax.dot_general(qh, kbs[hh], (((1,), (1,)), ((), ())), preferred_element_type=F32) * ATTN_SCALE
                    p = jnp.exp(s - lse_v[:, hh * NOPE:hh * NOPE + 1])
                    if diagonal:
                        p = jnp.where(_lower_triangle(tq), p, 0.0)
                    dp = lax.dot_general(dob, vb, (((1,), (1,)), ((), ())), preferred_element_type=F32)
                    ds = (p * (dp - delta_v[:, hh * NOPE:hh * NOPE + 1]) * ATTN_SCALE).astype(MX)
                    dks[hh] = dks[hh] + lax.dot_general(ds, qh, (((0,), (0,)), ((), ())), preferred_element_type=F32)
                    dv = dv + lax.dot_general(p.astype(MX), dob, (((0,), (0,)), ((), ())), preferred_element_type=F32)
                    dq_ref[pl.ds(q0, tq), hh * LANE:(hh + 1) * LANE] += jnp.dot(ds, kbs[hh], preferred_element_type=F32)
                return dks[0], dks[1], dv

            zero = jnp.zeros((tk, LANE), F32)
            carry = q_block(kj, (zero, zero, zero), True)
            dk0, dk1, dv = lax.fori_loop(kj + 1, nq, lambda qi, c: q_block(qi, c, False), carry)
            dk_ref[pl.ds(k0, tk), 0:LANE] = dk0
            dk_ref[pl.ds(k0, tk), LANE:2 * LANE] = dk1
            dv_ref[pl.ds(k0, tk), :] = dv
            return 0

        lax.fori_loop(0, nq, kv_step, 0)

    pair256 = pl.BlockSpec((S, 2 * LANE), lambda b, p: (b, p))
    pair128 = pl.BlockSpec((S, LANE), lambda b, p: (b, p))
    return _pcall(body, name=name, grid=(nb, npair),
                  in_specs=[pair256, pair256, pair128, pair128, pair128, pair128],
                  out_specs=[pair256, pair256, pair128],
                  out_shape=[_sds((T, NH * LANE), F32), _sds((T, NH * LANE), F32), _sds((T, NH * NOPE), F32)],
                  scratch=[pltpu.VMEM((S, LANE), F32)], comm=comm)(q, k, v, o, lse, do)


class _NoExchange:
    def __init__(self, weights):
        self.W, self.G = weights, {}

    def rider(self, name):
        return None

    def landed(self, name, outs):
        pass

    def grad(self, key, g):
        self.G[key] = g


def _run(plan, fn, *args, name, **kw):
    rider = plan.rider(name)
    if rider is None:
        return fn(*args, name=name, **kw)
    outs, landed = fn(*args, name=name, comm=rider, **kw)
    plan.landed(name, landed)
    return outs


def _layer_fwd_bwd(x, tgt, mod3, cos, sin, plan, P):
    nb = mod3.shape[0]
    W = plan.W
    h1, gu1, a1 = _run(plan, _ffn_up_normed, x, mod3, P["norm_ffn1"], W["ffn1_in"], sub=0, name="ffn1_up")
    f1, x1, h2 = _mm_resid_norm(a1, W["ffn1_out"], x, mod3, P["norm_mix"], sub=1, coef=0.5, name="ffn1_out")
    z = _run(plan, _mm, h2, W["w_in"], mode="nn", out_dtype=F32, name="mix_in", tn=1664)
    pooled, mixed, scaled = _pool_fwd(z, W["pool_grp"], P["pool_scale"], nb=nb, name="pool_fwd")
    br_pool = _mm(scaled, W["pool_proj"], mode="nn", out_dtype=F32, name="pool_proj")
    qn, kvn = _lat_fwd(z, P["q_a_norm"], P["kv_a_norm"], name="lat_fwd")
    qp = _mm(qn, W["q_up"], mode="nn", out_dtype=F32, name="q_up")
    kvp = _mm(kvn, W["kv_up"], mode="nn", out_dtype=F32, name="kv_up")
    q, k, v = _prep_fwd(qp, kvp, z, cos, sin, P["q_gain"], P["k_gain"], name="prep_fwd")
    attn, lse = _run(plan, _attn_fwd, q, k, v, nb=nb, name="attn_fwd")
    br_mla = _mm(attn, W["mla_proj"], mode="nn", out_dtype=F32, name="mla_proj")
    merged = _merge_fwd(z, br_pool, br_mla, name="merge_fwd")
    mo, x2, h3 = _mm_resid_norm(merged, W["w_out"], x1, mod3, P["norm_ffn2"], sub=2, coef=1.0, name="mix_out")
    gu2, a2 = _ffn_up(h3, W["ffn2_in"], name="ffn2_up")
    dy, df2, st_fin, loss = _mm_loss(a2, W["ffn2_out"], x2, tgt, mod3, name="ffn2_out_loss")

    plan.grad("ffn2_out", _dw(a2, df2, name="d_ffn2_out"))
    dgu2 = _ffn_dact(df2, gu2, W["ffn2_out"], name="ffn2_dact")
    plan.grad("ffn2_in", _ffn_dw_in(h3, dgu2, name="d_ffn2_in"))
    dx2, dmo, st3 = _run(plan, _ffn_dh_norm, dgu2, W["ffn2_in"], x2, dy, mo, mod3, P["norm_ffn2"], sub=2, coef=1.0,
                         name="d_ffn2_h")
    plan.grad("w_out", _dw(merged, dmo, name="d_mix_out"))
    dmerged = _mm(dmo, W["w_out"], mode="nt", out_dtype=F32, name="d_merged")
    dbr_pool, dbr_mla, dgates = _merge_bwd(z, br_pool, br_mla, dmerged, name="merge_bwd")
    plan.grad("pool_proj", _dw(scaled, dbr_pool, name="d_pool_proj"))
    dscaled = _mm(dbr_pool, W["pool_proj"], mode="nt", out_dtype=F32, name="d_pool_scaled")
    du_pool, d_pool_grp, d_pool_scale = _pool_bwd(dscaled, mixed, pooled, W["pool_grp"], P["pool_scale"], nb=nb, name="pool_bwd")
    plan.grad("mla_proj", _dw(attn, dbr_mla, name="d_mla_proj"))
    dattn = _mm(dbr_mla, W["mla_proj"], mode="nt", out_dtype=F32, name="d_attn")
    dq, dk, dv = _run(plan, _attn_bwd, q, k, v, attn, lse, dattn, nb=nb, name="attn_bwd")
    dqp, dkvp, dkr, st_prep = _prep_bwd(dq, dk, dv, qp, kvp, z, cos, sin, P["q_gain"], P["k_gain"], nb=nb, name="prep_bwd")
    plan.grad("q_up", _dw(qn, dqp, name="d_q_up"))
    dqn = _mm(dqp, W["q_up"], mode="nt", out_dtype=F32, name="d_qn")
    plan.grad("kv_up", _dw(kvn, dkvp, name="d_kv_up"))
    dkvn = _mm(dkvp, W["kv_up"], mode="nt", out_dtype=F32, name="d_kvn")
    dql, dkvl, st_q, st_kv = _lat_bwd(z, dqn, dkvn, P["q_a_norm"], P["kv_a_norm"], nb=nb, name="lat_bwd")
    dz = jnp.concatenate([du_pool, dql, dkvl, dkr, dgates], axis=1)
    plan.grad("w_in", _run(plan, _dw, h2, dz, name="d_mix_in", tn=1664))
    dx1, df1, st2 = _run(plan, _mm_nt_norm, dz, W["w_in"], x1, dx2, f1, mod3, P["norm_mix"], sub=1, coef=0.5, name="d_mix_h")
    plan.grad("ffn1_out", _run(plan, _dw, a1, df1, name="d_ffn1_out"))
    dgu1 = _run(plan, _ffn_dact, df1, gu1, W["ffn1_out"], name="ffn1_dact")
    plan.grad("ffn1_in", _run(plan, _ffn_dw_in, h1, dgu1, name="d_ffn1_in"))
    dh1 = _run(plan, _ffn_dh, dgu1, W["ffn1_in"], name="d_ffn1_h")
    grad_x, st1 = _run(plan, _bwd_block, x, dh1, dx1, None, mod3, P["norm_ffn1"], sub=0, coef=0.0, name="bwd_norm1")

    return loss, grad_x, (st1, st2, st3, st_fin, st_q, st_kv, st_prep, d_pool_scale), d_pool_grp


def _w_in_to_kernel(w):
    k = w.shape[0]
    zeros = lambda n: jnp.zeros((k, n), w.dtype)
    return jnp.concatenate([w[:, 0:1152], zeros(KR_LANE), w[:, 1152:1184], zeros(LANE - KR_LANE - ROPE), w[:, 1184:]], axis=1)


def _w_in_from_kernel(g):
    return jnp.concatenate([g[:, 0:1152], g[:, Z_KR + KR_LANE:Z_KR + KR_LANE + ROPE], g[:, Z_GP:]], axis=1)


def _q_up_to_kernel(w):
    k = w.shape[0]
    return jnp.pad(w.reshape(k, NH, NOPE + ROPE), ((0, 0), (0, 0), (0, LANE - NOPE - ROPE))).reshape(k, NH * LANE)


def _q_up_from_kernel(g):
    k = g.shape[0]
    return g.reshape(k, NH, LANE)[:, :, :NOPE + ROPE].reshape(k, NH * (NOPE + ROPE))


def _kv_up_to_kernel(w):
    k = w.shape[0]
    w3 = w.reshape(k, NH, 2 * NOPE)
    kpart = jnp.pad(w3[:, :, :NOPE], ((0, 0), (0, 0), (0, LANE - NOPE))).reshape(k, NH * LANE)
    return jnp.concatenate([kpart, w3[:, :, NOPE:].reshape(k, NH * NOPE)], axis=1)


def _kv_up_from_kernel(g):
    k = g.shape[0]
    kpart = g[:, :NH * LANE].reshape(k, NH, LANE)[:, :, :NOPE]
    vpart = g[:, NH * LANE:].reshape(k, NH, NOPE)
    return jnp.concatenate([kpart, vpart], axis=2).reshape(k, NH * 2 * NOPE)


def _gain_slab(nope, rope):
    return jnp.concatenate([nope, rope, jnp.zeros((1, LANE - NOPE - ROPE), nope.dtype)], axis=1)


def _rope_tables(positions):
    inv_freq = 10000.0 ** (-jnp.arange(0, ROPE, 2, dtype=F32) / ROPE)
    ang = positions.astype(F32)[:, None] * inv_freq
    ang = jnp.concatenate([ang, ang], axis=-1)
    t = positions.shape[0]
    cos = jnp.concatenate([jnp.ones((t, KR_LANE), F32), jnp.cos(ang), jnp.ones((t, LANE - KR_LANE - ROPE), F32)], axis=1)
    sin = jnp.concatenate([jnp.zeros((t, KR_LANE), F32), jnp.sin(ang), jnp.zeros((t, LANE - KR_LANE - ROPE), F32)], axis=1)
    return cos, sin


def _coords():
    return lax.axis_index("x"), lax.axis_index("y"), lax.axis_index("c")


HBM_SPEC = pl.BlockSpec(memory_space=pl.ANY)
VMEM_SPEC = pl.BlockSpec(memory_space=pltpu.VMEM)


def _gather8(v, *, name, comm=None):
    r, c = v.shape

    def body(v_ref, out_ref, send_sems, recv_sems, local_sem):
        x, y, cc = _coords()
        me = 4 * x + 2 * y + cc
        mine = pltpu.make_async_copy(v_ref, out_ref.at[me], local_sem)
        mine.start()
        copies = []
        for kk in range(1, N_DEV):
            peer = (x ^ (kk >> 2), y ^ ((kk >> 1) & 1), cc ^ (kk & 1))
            cp = pltpu.make_async_remote_copy(src_ref=v_ref, dst_ref=out_ref.at[me], send_sem=send_sems.at[kk - 1],
                                              recv_sem=recv_sems.at[kk - 1], device_id=peer, device_id_type=MESH)
            cp.start()
            copies.append(cp)
        for kk in range(1, N_DEV):
            peer_slot = me ^ kk
            pltpu.make_async_remote_copy(src_ref=v_ref, dst_ref=out_ref.at[peer_slot], send_sem=send_sems.at[kk - 1],
                                         recv_sem=recv_sems.at[kk - 1], device_id=(x, y, cc), device_id_type=MESH).wait_recv()
        for cp in copies:
            cp.wait_send()
        mine.wait()

    return _pcall(body, name=name, out_shape=_sds((N_DEV, r, c), v.dtype), in_specs=[VMEM_SPEC], out_specs=VMEM_SPEC,
                  scratch=[pltpu.SemaphoreType.DMA((N_DEV - 1,)), pltpu.SemaphoreType.DMA((N_DEV - 1,)), pltpu.SemaphoreType.DMA],
                  comm=comm)(v)


CHIP_RELS = ((1, 0), (0, 1), (1, 1))


def _comm_only(comm, *, name):
    _, outs = _pcall(lambda: None, name=name, out_shape=[], in_specs=[], out_specs=[], comm=comm)()
    return outs


def _gather_comm(shards):
    n = len(shards)

    def ici(ins, outs, sems, a, j, x, y, cc):
        half = ins[a].shape[0] // 2
        mine = pl.ds(cc * half, half)
        dx, dy = CHIP_RELS[j]
        return pltpu.make_async_remote_copy(src_ref=ins[a].at[mine], dst_ref=outs[a].at[2 * x + y, mine],
                                            send_sem=sems[0].at[a, j], recv_sem=sems[1].at[a, j],
                                            device_id=(x ^ dx, y ^ dy, cc), device_id_type=MESH)

    def d2d(ins, outs, sems, a, j, x, y, cc, half_of):
        half = ins[a].shape[0] // 2
        dx, dy = CHIP_RELS[j]
        landed = outs[a].at[2 * (x ^ dx) + (y ^ dy), pl.ds(half_of * half, half)]
        return pltpu.make_async_remote_copy(src_ref=landed, dst_ref=landed, send_sem=sems[2].at[a, j], recv_sem=sems[3].at[a, j],
                                            device_id=(x, y, 1 - cc), device_id_type=MESH)

    def own(ins, outs, sems, a, x, y):
        return pltpu.make_async_copy(ins[a], outs[a].at[2 * x + y], sems[4].at[a])

    def start(ins, outs, sems):
        x, y, cc = _coords()
        for a in range(n):
            own(ins, outs, sems, a, x, y).start()
            for j in range(3):
                ici(ins, outs, sems, a, j, x, y, cc).start()

    def finish(ins, outs, sems):
        x, y, cc = _coords()
        for a in range(n):
            for j in range(3):
                ici(ins, outs, sems, a, j, x, y, cc).wait_recv()
                d2d(ins, outs, sems, a, j, x, y, cc, cc).start()
        for a in range(n):
            for j in range(3):
                d2d(ins, outs, sems, a, j, x, y, cc, 1 - cc).wait_recv()
        for a in range(n):
            for j in range(3):
                ici(ins, outs, sems, a, j, x, y, cc).wait_send()
                d2d(ins, outs, sems, a, j, x, y, cc, cc).wait_send()
            own(ins, outs, sems, a, x, y).wait()

    dma = pltpu.SemaphoreType.DMA
    return _Comm(shards, [_sds((N_CHIP,) + s.shape, s.dtype) for s in shards],
                 [dma((n, 3)), dma((n, 3)), dma((n, 3)), dma((n, 3)), dma((n,))], start, finish)


def _swap_comm(parts):
    n = len(parts)

    def copy(ins, outs, sems, a):
        x, y, cc = _coords()
        half = ins[a].shape[1] // 2
        return pltpu.make_async_remote_copy(src_ref=ins[a].at[:, pl.ds((1 - cc) * half, half)], dst_ref=outs[a],
                                            send_sem=sems[0].at[a], recv_sem=sems[1].at[a], device_id=(x, y, 1 - cc),
                                            device_id_type=MESH)

    def start(ins, outs, sems):
        for a in range(n):
            copy(ins, outs, sems, a).start()

    def finish(ins, outs, sems):
        for a in range(n):
            copy(ins, outs, sems, a).wait()

    dma = pltpu.SemaphoreType.DMA
    return _Comm(parts, [_sds((p.shape[0], p.shape[1] // 2, p.shape[2]), p.dtype) for p in parts], [dma((n,)), dma((n,))],
                 start, finish)


def _add_half(full, other, cidx, *, name):
    nch, r, c = full.shape
    half = r // 2
    tr = _pick_rows(half)
    nbk = half // tr
    grid_spec = pltpu.PrefetchScalarGridSpec(
        num_scalar_prefetch=1, grid=(nch, nbk),
        in_specs=[pl.BlockSpec((1, tr, c), lambda j, i, cref: (j, cref[0] * nbk + i, 0)),
                  pl.BlockSpec((1, tr, c), lambda j, i, cref: (j, i, 0))],
        out_specs=pl.BlockSpec((1, tr, c), lambda j, i, cref: (j, i, 0)))

    def body(cref, a_ref, b_ref, o_ref):
        o_ref[...] = (a_ref[...] + b_ref[...]).astype(o_ref.dtype)

    return _pcall(body, name=name, out_shape=_sds((nch, half, c), MX), grid_spec=grid_spec)(cidx, full, other)


def _pick_rows(rows, target=512):
    best = None
    for t in range(16, min(rows, target) + 1, 16):
        if rows % t == 0:
            best = t
    return rows if best is None else best


def _exchange_comm(parts):
    n = len(parts)

    def send(ins, outs, sems, a, j, x, y, cc):
        dx, dy = CHIP_RELS[j]
        return pltpu.make_async_remote_copy(src_ref=ins[a].at[2 * (x ^ dx) + (y ^ dy)], dst_ref=outs[a].at[2 * x + y],
                                            send_sem=sems[0].at[a, j], recv_sem=sems[1].at[a, j],
                                            device_id=(x ^ dx, y ^ dy, cc), device_id_type=MESH)

    def landing(ins, outs, sems, a, j, x, y, cc):
        dx, dy = CHIP_RELS[j]
        peer_chip = 2 * (x ^ dx) + (y ^ dy)
        return pltpu.make_async_remote_copy(src_ref=ins[a].at[peer_chip], dst_ref=outs[a].at[peer_chip], send_sem=sems[0].at[a, j],
                                            recv_sem=sems[1].at[a, j], device_id=(x, y, cc), device_id_type=MESH)

    def own(ins, outs, sems, a, x, y):
        return pltpu.make_async_copy(ins[a].at[2 * x + y], outs[a].at[2 * x + y], sems[2].at[a])

    def start(ins, outs, sems):
        x, y, cc = _coords()
        for a in range(n):
            own(ins, outs, sems, a, x, y).start()
            for j in range(3):
                send(ins, outs, sems, a, j, x, y, cc).start()

    def finish(ins, outs, sems):
        x, y, cc = _coords()
        for a in range(n):
            for j in range(3):
                landing(ins, outs, sems, a, j, x, y, cc).wait_recv()
        for a in range(n):
            for j in range(3):
                send(ins, outs, sems, a, j, x, y, cc).wait_send()
            own(ins, outs, sems, a, x, y).wait()

    dma = pltpu.SemaphoreType.DMA
    return _Comm(parts, [_sds(p.shape, p.dtype) for p in parts], [dma((n, 3)), dma((n, 3)), dma((n,))], start, finish)


def _sum_chips(q, cidx, *, name):
    nch, h, c = q.shape
    tr = _pick_rows(h)
    nbk = h // tr
    grid_spec = pltpu.PrefetchScalarGridSpec(
        num_scalar_prefetch=1, grid=(nbk,),
        in_specs=[pl.BlockSpec((nch, tr, c), lambda i, cref: (0, i, 0))],
        out_specs=pl.BlockSpec((tr, c), lambda i, cref: (cref[0] * nbk + i, 0)))

    def body(cref, q_ref, o_ref):
        acc = q_ref[0].astype(F32) + q_ref[1].astype(F32)
        acc = acc + q_ref[2].astype(F32)
        o_ref[...] = acc + q_ref[3].astype(F32)

    return _pcall(body, name=name, out_shape=_sds((2 * h, c), F32), grid_spec=grid_spec)(cidx, q)


def _join_comm(fulls):
    n = len(fulls)

    def half_copy(outs, sems, a, which):
        x, y, cc = _coords()
        h = outs[a].shape[0] // 2
        rows = outs[a].at[pl.ds((cc if which == 0 else 1 - cc) * h, h)]
        return pltpu.make_async_remote_copy(src_ref=rows, dst_ref=rows, send_sem=sems[0].at[a], recv_sem=sems[1].at[a],
                                            device_id=(x, y, 1 - cc), device_id_type=MESH)

    def start(ins, outs, sems):
        for a in range(n):
            half_copy(outs, sems, a, 0).start()

    def finish(ins, outs, sems):
        for a in range(n):
            half_copy(outs, sems, a, 1).wait_recv()
        for a in range(n):
            half_copy(outs, sems, a, 0).wait_send()

    dma = pltpu.SemaphoreType.DMA
    return _Comm(fulls, [_sds(p.shape, p.dtype) for p in fulls], [dma((n,)), dma((n,))], start, finish,
                 aliases={a: a for a in range(n)})


def _ada_prologue(c, w, b, *, name, comm=None):
    nb, dm = c.shape
    n = w.shape[1]

    def body(c_ref, w_ref, b_ref, call_ref, land_ref, cpad, mod_s, g_send, g_recv, m_send, m_recv):
        x, y, cc = _coords()
        me = 4 * x + 2 * y + cc
        chip = 2 * x + y
        cpad[...] = jnp.zeros_like(cpad)
        cpad[0:nb, :] = c_ref[...]
        copies = []
        for kk in range(1, N_DEV):
            peer = (x ^ (kk >> 2), y ^ ((kk >> 1) & 1), cc ^ (kk & 1))
            cp = pltpu.make_async_remote_copy(src_ref=cpad, dst_ref=call_ref.at[me], send_sem=g_send.at[kk - 1],
                                              recv_sem=g_recv.at[kk - 1], device_id=peer, device_id_type=MESH)
            cp.start()
            copies.append(cp)
        call_ref[me] = cpad[...]
        for kk in range(1, N_DEV):
            pltpu.make_async_remote_copy(src_ref=cpad, dst_ref=call_ref.at[me ^ kk], send_sem=g_send.at[kk - 1],
                                         recv_sem=g_recv.at[kk - 1], device_id=(x, y, cc), device_id_type=MESH).wait_recv()
        cv = call_ref[...].reshape(N_DEV * SUBLANE, dm)
        act = (cv * _sigmoid(cv)).astype(MX)
        mod = jnp.dot(act, w_ref[...].astype(MX), preferred_element_type=F32) + b_ref[...]
        mod_s[...] = mod.reshape(N_DEV, SUBLANE, n)
        for j, (dx, dy) in enumerate(CHIP_RELS):
            cp = pltpu.make_async_remote_copy(src_ref=mod_s.at[4 * (x ^ dx) + 2 * (y ^ dy) + cc], dst_ref=land_ref.at[chip],
                                              send_sem=m_send.at[j], recv_sem=m_recv.at[j],
                                              device_id=(x ^ dx, y ^ dy, cc), device_id_type=MESH)
            cp.start()
            copies.append(cp)
        land_ref[chip] = mod_s[me]
        for j, (dx, dy) in enumerate(CHIP_RELS):
            pltpu.make_async_remote_copy(src_ref=mod_s.at[me], dst_ref=land_ref.at[2 * (x ^ dx) + (y ^ dy)], send_sem=m_send.at[j],
                                         recv_sem=m_recv.at[j], device_id=(x, y, cc), device_id_type=MESH).wait_recv()
        for cp in copies:
            cp.wait_send()

    dma = pltpu.SemaphoreType.DMA
    return _pcall(body, name=name, in_specs=[VMEM_SPEC] * 3, out_specs=[VMEM_SPEC] * 2,
                  out_shape=[_sds((N_DEV, SUBLANE, dm), F32), _sds((N_CHIP, SUBLANE, n), F32)],
                  scratch=[pltpu.VMEM((SUBLANE, dm), F32), pltpu.VMEM((N_DEV, SUBLANE, n), F32),
                           dma((N_DEV - 1,)), dma((N_DEV - 1,)), dma((3,)), dma((3,))], comm=comm)(c, w, b)


def _ada_bwd(c_all, dmod_cols, *, name):
    m, kdim = c_all.shape
    n = dmod_cols.shape[1]
    tn = _pick(n, 1152)

    def body(c_ref, d_ref, o_ref):
        cv = c_ref[...]
        act = (cv * _sigmoid(cv)).astype(MX)
        o_ref[...] = lax.dot_general(act, d_ref[...].astype(MX), (((0,), (0,)), ((), ())), preferred_element_type=F32)

    return _pcall(body, name=name, out_shape=_sds((kdim, n), F32), grid=(n // tn,),
                  in_specs=[pl.BlockSpec((m, kdim), lambda j: (0, 0)), pl.BlockSpec((m, tn), lambda j: (0, j))],
                  out_specs=pl.BlockSpec((kdim, tn), lambda j: (0, j)))(c_all, dmod_cols)


SLAB_W = 1024
RED_ROWS = 8


LOSS_LANE = SLAB_W - LANE


def _pack_stats(st1, st2, st3, st_fin, st_q, st_kv, st_prep, d_pool_scale, loss8, *, name):
    nb = st1.shape[0]
    dm_rows = -(-9 * nb // SUBLANE) * SUBLANE

    def body(s1, s2, s3, sf, sq, skv, sp, sps, loss_ref, o_ref):
        o_ref[...] = jnp.zeros_like(o_ref)
        for s in range(nb):
            rows = [s1[s, 0:1, :], s1[s, 1:2, :], s2[s, 3:4, :], s2[s, 0:1, :], s2[s, 1:2, :], s3[s, 3:4, :], s3[s, 0:1, :],
                    s3[s, 1:2, :], sf[s, 0:1, :]]
            for k, row in enumerate(rows):
                o_ref[9 * s + k:9 * s + k + 1, :] = row

        def over_seq(ref, r):
            acc = ref[0, r:r + 1, :]
            for s in range(1, nb):
                acc = acc + ref[s, r:r + 1, :]
            return acc

        o_ref[dm_rows + 0:dm_rows + 1, :] = over_seq(s1, 2)
        o_ref[dm_rows + 1:dm_rows + 2, :] = over_seq(s2, 2)
        o_ref[dm_rows + 2:dm_rows + 3, :] = over_seq(s3, 2)
        o_ref[dm_rows + 3:dm_rows + 4, 0:POOL_W] = over_seq(sps, 0)
        o_ref[dm_rows + 4:dm_rows + 5, 0:QL] = over_seq(sq, 0)
        o_ref[dm_rows + 5:dm_rows + 6, 0:KVL] = over_seq(skv, 0)
        o_ref[dm_rows + 6:dm_rows + 7, 0:LANE] = over_seq(sp, 0)
        o_ref[dm_rows + 7:dm_rows + 8, 0:LANE] = over_seq(sp, 1)
        o_ref[dm_rows + 7:dm_rows + 8, LOSS_LANE:] = loss_ref[0:1, :]

    return _pcall(body, name=name, out_shape=_sds((dm_rows + RED_ROWS, SLAB_W), F32), in_specs=[VMEM_SPEC] * 9,
                  out_specs=VMEM_SPEC)(st1, st2, st3, st_fin, st_q, st_kv, st_prep, d_pool_scale, loss8)


def _small_allreduce(slab, pool, *, name, comm=None):
    rows, w = slab.shape
    dm = rows - RED_ROWS
    prow, pw = pool.shape
    crow = RED_ROWS + 2 * dm

    def body(slab_ref, pool_ref, red_ref, dmod_ref, ptot_ref, sib_slab, sib_pool, chip_slab, chip_pool, land_slab, land_pool,
             a_send, a_recv, b_send, b_recv):
        x, y, cc = _coords()
        chip = 2 * x + y
        sib = (x, y, 1 - cc)
        to_sib = [pltpu.make_async_remote_copy(src_ref=slab_ref, dst_ref=sib_slab, send_sem=a_send.at[0], recv_sem=a_recv.at[0],
                                               device_id=sib, device_id_type=MESH),
                  pltpu.make_async_remote_copy(src_ref=pool_ref, dst_ref=sib_pool, send_sem=a_send.at[1], recv_sem=a_recv.at[1],
                                               device_id=sib, device_id_type=MESH)]
        for cp in to_sib:
            cp.start()
        for cp in to_sib:
            cp.wait()
        mine_dm, theirs_dm = slab_ref[0:dm, :], sib_slab[0:dm, :]
        chip_slab[0:RED_ROWS, :] = slab_ref[dm:, :] + sib_slab[dm:, :]
        chip_slab[RED_ROWS:RED_ROWS + dm, :] = jnp.where(cc == 0, mine_dm, theirs_dm)
        chip_slab[RED_ROWS + dm:, :] = jnp.where(cc == 0, theirs_dm, mine_dm)
        chip_pool[...] = pool_ref[...] + sib_pool[...]

        sends = []
        for j, (dx, dy) in enumerate(CHIP_RELS):
            peer = (x ^ dx, y ^ dy, cc)
            sends.append(pltpu.make_async_remote_copy(src_ref=chip_slab, dst_ref=land_slab.at[chip], send_sem=b_send.at[j, 0],
                                                      recv_sem=b_recv.at[j, 0], device_id=peer, device_id_type=MESH))
            sends.append(pltpu.make_async_remote_copy(src_ref=chip_pool, dst_ref=land_pool.at[chip], send_sem=b_send.at[j, 1],
                                                      recv_sem=b_recv.at[j, 1], device_id=peer, device_id_type=MESH))
        for cp in sends:
            cp.start()
        land_slab[chip] = chip_slab[...]
        land_pool[chip] = chip_pool[...]
        for j, (dx, dy) in enumerate(CHIP_RELS):
            peer_chip = 2 * (x ^ dx) + (y ^ dy)
            pltpu.make_async_remote_copy(src_ref=chip_slab, dst_ref=land_slab.at[peer_chip], send_sem=b_send.at[j, 0],
                                         recv_sem=b_recv.at[j, 0], device_id=(x, y, cc), device_id_type=MESH).wait_recv()
            pltpu.make_async_remote_copy(src_ref=chip_pool, dst_ref=land_pool.at[peer_chip], send_sem=b_send.at[j, 1],
                                         recv_sem=b_recv.at[j, 1], device_id=(x, y, cc), device_id_type=MESH).wait_recv()
        red = land_slab[0, 0:RED_ROWS, :]
        ptot = land_pool[0]
        for ch in range(1, N_CHIP):
            red = red + land_slab[ch, 0:RED_ROWS, :]
            ptot = ptot + land_pool[ch]
        red_ref[...] = red
        ptot_ref[...] = ptot
        for ch in range(N_CHIP):
            dmod_ref[2 * dm * ch:2 * dm * (ch + 1), :] = land_slab[ch, RED_ROWS:, :]
        for cp in sends:
            cp.wait_send()

    dma = pltpu.SemaphoreType.DMA
    return _pcall(body, name=name, in_specs=[VMEM_SPEC, VMEM_SPEC], out_specs=[VMEM_SPEC] * 3,
                  out_shape=[_sds((RED_ROWS, w), F32), _sds((N_DEV * dm, w), F32), _sds((prow, pw), F32)],
                  scratch=[pltpu.VMEM((rows, w), F32), pltpu.VMEM((prow, pw), F32), pltpu.VMEM((crow, w), F32),
                           pltpu.VMEM((prow, pw), F32), pltpu.VMEM((N_CHIP, crow, w), F32), pltpu.VMEM((N_CHIP, prow, pw), F32),
                           dma((2,)), dma((2,)), dma((3, 2)), dma((3, 2))], comm=comm)(slab, pool)


def _adamw_math(w, g, m, v):
    mn = ADAM_B1 * m + (1.0 - ADAM_B1) * g
    vn = ADAM_B2 * v + (1.0 - ADAM_B2) * (g * g)
    bc1 = 1.0 / (1.0 - ADAM_B1 ** ADAM_STEP)
    bc2 = 1.0 / (1.0 - ADAM_B2 ** ADAM_STEP)
    return -ADAM_LR * ((mn * bc1) / (jnp.sqrt(vn * bc2) + ADAM_EPS) + ADAM_WD * w), mn, vn


def _small_update(red, dmod_all, pool_total, nb, params, *, name):
    names = list(SMALL)
    dm = dmod_all.shape[0] // N_DEV

    def grad_of(nm, red_ref, dmod_ref, ptot_ref):
        if nm == "b_ada":
            acc = None
            for d in range(N_DEV):
                for s in range(nb):
                    blk = dmod_ref[d * dm + 9 * s:d * dm + 9 * s + 9, :]
                    acc = blk if acc is None else acc + blk
            return jnp.concatenate([acc[k:k + 1, :] for k in range(9)], axis=1)
        if nm == "pool_grp":
            return ptot_ref[...]
        row, lo, n = {"norm_ffn1": (0, 0, D), "norm_mix": (1, 0, D), "norm_ffn2": (2, 0, D), "pool_scale": (3, 0, POOL_W),
                      "q_a_norm": (4, 0, QL), "kv_a_norm": (5, 0, KVL), "q_norm_nope": (6, 0, NOPE),
                      "q_norm_rope": (6, NOPE, ROPE), "k_norm_nope": (7, 0, NOPE), "k_norm_rope": (7, KR_LANE, ROPE)}[nm]
        return red_ref[row:row + 1, lo:lo + n]

    def body(*refs):
        red_ref, dmod_ref, ptot_ref = refs[:3]
        ins = refs[3:3 + 3 * len(names)]
        outs = refs[3 + 3 * len(names):]
        outs[4 * len(names)][...] = red_ref[RED_ROWS - 1:RED_ROWS, LOSS_LANE:]
        for i, nm in enumerate(names):
            g = grad_of(nm, red_ref, dmod_ref, ptot_ref)
            d, mn, vn = _adamw_math(ins[3 * i][...], g, ins[3 * i + 1][...], ins[3 * i + 2][...])
            outs[4 * i][...] = g
            outs[4 * i + 1][...] = d
            outs[4 * i + 2][...] = mn
            outs[4 * i + 3][...] = vn

    flat_in = [a for nm in names for a in params[nm]]
    out_shape = [_sds(params[nm][0].shape, F32) for nm in names for _ in range(4)] + [_sds((1, LANE), F32)]
    res = _pcall(body, name=name, in_specs=[VMEM_SPEC] * (3 + len(flat_in)), out_specs=[VMEM_SPEC] * len(out_shape),
                 out_shape=out_shape)(red, dmod_all, pool_total, *flat_in)
    return {nm: tuple(res[4 * i:4 * i + 4]) for i, nm in enumerate(names)}, res[-1]


def _adamw(w, g, m, v, *, name, comm=None):
    r, c = w.shape
    tr = _pick_rows(r, 256)
    tc = c if tr < r else _pick(c, 256)
    spec = pl.BlockSpec((tr, tc), lambda i, j: (i, j))
    bc1 = 1.0 / (1.0 - ADAM_B1 ** ADAM_STEP)
    bc2 = 1.0 / (1.0 - ADAM_B2 ** ADAM_STEP)

    def body(w_ref, g_ref, m_ref, v_ref, d_ref, mo_ref, vo_ref):
        gv = g_ref[...]
        mn = ADAM_B1 * m_ref[...] + (1.0 - ADAM_B1) * gv
        vn = ADAM_B2 * v_ref[...] + (1.0 - ADAM_B2) * (gv * gv)
        mo_ref[...] = mn
        vo_ref[...] = vn
        d_ref[...] = -ADAM_LR * ((mn * bc1) / (jnp.sqrt(vn * bc2) + ADAM_EPS) + ADAM_WD * w_ref[...])

    out = _sds((r, c), F32)
    return _pcall(body, name=name, out_shape=[out, out, out], grid=(r // tr, c // tc), in_specs=[spec] * 4, out_specs=[spec] * 3,
                  comm=comm)(w, g, m, v)


BIG = ("w_ffn1_in", "w_ffn1_out", "w_in", "w_pool_proj", "w_q_up", "w_kv_up", "w_mla_proj", "w_out", "w_ffn2_in", "w_ffn2_out")
ROW_SHARDED = ("w_ffn1_out", "w_out", "w_ffn2_out")
KERNEL_NAME = {"w_ffn1_in": "ffn1_in", "w_ffn1_out": "ffn1_out", "w_in": "w_in", "w_pool_proj": "pool_proj", "w_q_up": "q_up",
               "w_kv_up": "kv_up", "w_mla_proj": "mla_proj", "w_out": "w_out", "w_ffn2_in": "ffn2_in", "w_ffn2_out": "ffn2_out"}
WEIGHTS = ("w_ada", "b_ada", "norm_ffn1", "w_ffn1_in", "w_ffn1_out", "norm_mix", "w_in", "pool_grp", "pool_scale", "w_pool_proj",
           "q_a_norm", "w_q_up", "kv_a_norm", "w_kv_up", "q_norm_nope", "q_norm_rope", "k_norm_nope", "k_norm_rope", "w_mla_proj",
           "w_out", "norm_ffn2", "w_ffn2_in", "w_ffn2_out")
SMALL = ("b_ada", "norm_ffn1", "norm_mix", "pool_grp", "pool_scale", "q_a_norm", "kv_a_norm", "q_norm_nope", "q_norm_rope",
         "k_norm_nope", "k_norm_rope", "norm_ffn2")
SMALL_SEQ = tuple(n for n in SMALL if n != "pool_grp")
SLAB_W = 1024


def _assemble(name, stacked):
    if name in ROW_SHARDED:
        return stacked.reshape(stacked.shape[0] * stacked.shape[1], stacked.shape[2])
    return jnp.transpose(stacked, (1, 0, 2)).reshape(stacked.shape[1], stacked.shape[0] * stacked.shape[2])


def _split(name, full):
    if name in ROW_SHARDED:
        return full.reshape(N_CHIP, full.shape[0] // N_CHIP, full.shape[1])
    return jnp.transpose(full.reshape(full.shape[0], N_CHIP, full.shape[1] // N_CHIP), (1, 0, 2))


GU = ("w_ffn1_in", "w_ffn2_in")
NARROW = ("w_in", "w_q_up")


def _to_rows(v, width=SLAB_W):
    flat = v.reshape(-1)
    rows = -(-flat.shape[0] // width)
    return jnp.pad(flat, (0, rows * width - flat.shape[0])).reshape(rows, width)


def _pack_small(parts, names=SMALL):
    rows, spans, at = [], {}, 0
    for name in names:
        r = _to_rows(parts[name])
        spans[name] = (at, parts[name].size, parts[name].shape)
        rows.append(r)
        at += r.shape[0]
    pad = (-at) % SUBLANE
    if pad:
        rows.append(jnp.zeros((pad, SLAB_W), F32))
    return jnp.concatenate(rows, axis=0), spans


def _unpack_small(slab, spans):
    out = {}
    for name, (at, size, shape) in spans.items():
        nrow = -(-size // SLAB_W)
        out[name] = slab[at:at + nrow].reshape(-1)[:size].reshape(shape)
    return out


MIX_SMALL = ("w_out", "w_pool_proj", "w_mla_proj", "w_q_up", "w_kv_up")
RIDES = {
    "ada_prologue": (("gather", ("w_ffn1_in",)),),
    "ffn1_up": (("gather", ("w_ffn1_out", "w_in")),),
    "mix_in": (("gather", MIX_SMALL),),
    "attn_fwd": (("gather", ("w_ffn2_in", "w_ffn2_out")),),
    "d_ffn2_h": (("swap", ("w_ffn2_out", "w_ffn2_in")),),
    "attn_bwd": (("exchange", ("w_ffn2_out", "w_ffn2_in")),),
    "d_mix_in": (("swap", MIX_SMALL),),
    "d_mix_h": (("exchange", MIX_SMALL), ("swap", ("w_in",))),
    "d_ffn1_out": (("exchange", ("w_in",)),),
    "ffn1_dact": (("swap", ("w_ffn1_out",)),),
    "d_ffn1_in": (("exchange", ("w_ffn1_out",)),),
    "d_ffn1_h": (("swap", ("w_ffn1_in",)),),
    "bwd_norm1": (("exchange", ("w_ffn1_in",)),),
}


def _both(comms):
    if len(comms) == 1:
        return comms[0]
    ins, outs, sems, aliases, spans = [], [], [], {}, []
    for c in comms:
        spans.append((len(ins), len(c.ins), len(outs), len(c.out_shapes), len(sems), len(c.sems)))
        aliases.update({len(ins) + i: len(outs) + o for i, o in c.aliases.items()})
        ins, outs, sems = ins + c.ins, outs + c.out_shapes, sems + c.sems

    def each(which):
        def run(i_, o_, s_):
            for c, (ia, ni, oa, no, sa, ns) in zip(comms, spans):
                getattr(c, which)(i_[ia:ia + ni], o_[oa:oa + no], s_[sa:sa + ns])
        return run

    return _Comm(ins, outs, sems, each("start"), each("finish"), aliases)


class _ExchangePlan:
    def __init__(self, shards, cidx):
        self.shards, self.cidx = shards, cidx
        self.W, self.G, self.parts, self.pre, self.reduced = {}, {}, {}, {}, {}

    def grad(self, key, g):
        self.G[key] = g

    def rider(self, name):
        comms = []
        for kind, names in RIDES.get(name, ()):
            if kind == "gather":
                comms.append(_gather_comm([self.shards[n] for n in names]))
            elif kind == "swap":
                for n in names:
                    self.parts[n] = self._stacked(n)
                comms.append(_swap_comm([self.parts[n] for n in names]))
            else:
                comms.append(_exchange_comm([self.pre[n] for n in names]))
        return _both(comms) if comms else None

    def landed(self, name, outs):
        at = 0
        for kind, names in RIDES[name]:
            for n, o in zip(names, outs[at:at + len(names)]):
                if kind == "gather":
                    self.W[KERNEL_NAME[n]] = self._to_kernel(n, o)
                elif kind == "swap":
                    self.pre[n] = _add_half(self.parts[n], o, self.cidx, name="rs_add_" + KERNEL_NAME[n])
                else:
                    self.reduced[n] = _sum_chips(o, self.cidx, name="rs_sum_" + KERNEL_NAME[n])
            at += len(names)

    @staticmethod
    def _to_kernel(n, stacked):
        if n in GU:
            return stacked
        full = _assemble(n, stacked)
        return {"w_in": _w_in_to_kernel, "w_q_up": _q_up_to_kernel, "w_kv_up": _kv_up_to_kernel}.get(n, lambda w: w)(full)

    def _stacked(self, n):
        g = self.G[KERNEL_NAME[n]]
        if n in GU:
            return g
        full = {"w_in": _w_in_from_kernel, "w_q_up": _q_up_from_kernel, "w_kv_up": _kv_up_from_kernel}.get(n, lambda w: w)(g)
        return _split(n, full)


def kernel(x, c, positions, w_ada, b_ada, norm_ffn1, w_ffn1_in, w_ffn1_out, norm_mix, w_in, pool_grp, pool_scale, w_pool_proj, q_a_norm, w_q_up, kv_a_norm, w_kv_up, q_norm_nope, q_norm_rope, k_norm_nope, k_norm_rope, w_mla_proj, w_out, norm_ffn2, w_ffn2_in, w_ffn2_out, loss_target, m_w_ada, m_b_ada, m_norm_ffn1, m_w_ffn1_in, m_w_ffn1_out, m_norm_mix, m_w_in, m_pool_grp, m_pool_scale, m_w_pool_proj, m_q_a_norm, m_w_q_up, m_kv_a_norm, m_w_kv_up, m_q_norm_nope, m_q_norm_rope, m_k_norm_nope, m_k_norm_rope, m_w_mla_proj, m_w_out, m_norm_ffn2, m_w_ffn2_in, m_w_ffn2_out, v_w_ada, v_b_ada, v_norm_ffn1, v_w_ffn1_in, v_w_ffn1_out, v_norm_mix, v_w_in, v_pool_grp, v_pool_scale, v_w_pool_proj, v_q_a_norm, v_w_q_up, v_kv_a_norm, v_w_kv_up, v_q_norm_nope, v_q_norm_rope, v_k_norm_nope, v_k_norm_rope, v_w_mla_proj, v_w_out, v_norm_ffn2, v_w_ffn2_in, v_w_ffn2_out):
    args = dict(locals())
    wts = {n: args[n][0] for n in WEIGHTS}
    mom = {n: args["m_" + n][0] for n in WEIGHTS}
    var = {n: args["v_" + n][0] for n in WEIGHTS}
    nb, seq, dm = x.shape
    tokens = nb * seq
    xi, yi, ci = _coords()
    chip = 2 * xi + yi
    dev = 4 * xi + 2 * yi + ci

    cidx = ci.astype(jnp.int32).reshape(1)
    plan = _ExchangePlan({n: wts[n].astype(MX) for n in BIG}, cidx)
    plan.W["pool_grp"] = wts["pool_grp"].astype(MX)

    ncol = w_ada.shape[2]
    b_cols = lax.dynamic_slice_in_dim(wts["b_ada"].reshape(1, -1), chip * ncol, ncol, axis=1)
    c_slots, mod_slots = _run(plan, _ada_prologue, c, wts["w_ada"], b_cols, name="ada_prologue")
    c_all = c_slots[:, :nb].reshape(N_DEV * nb, dm)
    mod3 = jnp.transpose(mod_slots[:, :nb], (1, 0, 2)).reshape(nb, 9, dm)
    P = {"norm_ffn1": wts["norm_ffn1"].reshape(1, dm), "norm_mix": wts["norm_mix"].reshape(1, dm),
         "norm_ffn2": wts["norm_ffn2"].reshape(1, dm), "pool_scale": wts["pool_scale"].reshape(1, POOL_W),
         "q_a_norm": wts["q_a_norm"].reshape(1, QL), "kv_a_norm": wts["kv_a_norm"].reshape(1, KVL),
         "q_gain": _gain_slab(wts["q_norm_nope"].reshape(1, NOPE), wts["q_norm_rope"].reshape(1, ROPE)),
         "k_gain": _gain_slab(wts["k_norm_nope"].reshape(1, NOPE), wts["k_norm_rope"].reshape(1, ROPE))}
    cos, sin = _rope_tables(positions.reshape(tokens))

    loss8, grad_x, stats, d_pool_grp = _layer_fwd_bwd(x.reshape(tokens, dm), loss_target.reshape(tokens, dm), mod3, cos, sin, plan, P)

    slab = _pack_stats(*stats, loss8, name="pack_stats")
    red, dmod_rows, pool_total = _run(plan, _small_allreduce, slab, d_pool_grp.reshape(POOL_G * LANE, LANE), name="small_allreduce")
    joined = _comm_only(_join_comm([plan.reduced[n] for n in BIG]), name="rs_join_halves")
    grads = {n: g for n, g in zip(BIG, joined)}
    dm_rows = dmod_rows.shape[0] // N_DEV
    dmod_all = dmod_rows.reshape(N_DEV, dm_rows, SLAB_W)[:, :9 * nb].reshape(N_DEV * nb, 9 * dm)
    dmod_cols = lax.dynamic_slice_in_dim(dmod_all, chip * ncol, ncol, axis=1)
    grads["w_ada"] = _ada_bwd(c_all, dmod_cols, name="ada_bwd")

    delta, new_m, new_v = {}, {}, {}
    as2d = lambda a: a.reshape(POOL_G * LANE, LANE) if a.ndim == 4 else a.reshape(1, -1)
    upd, loss_row = _small_update(red, dmod_rows, pool_total, nb,
                                  {n: tuple(as2d(args[p + n]) for p in ("", "m_", "v_")) for n in SMALL}, name="small_update")
    loss = loss_row[0, 0]
    for n in SMALL:
        grads[n], delta[n], new_m[n], new_v[n] = upd[n]
    for n in ("w_ada",) + BIG:
        if n in NARROW:
            res = _adamw(wts[n].T, grads[n].T, mom[n].T, var[n].T, name="adamw_" + n)
            delta[n], new_m[n], new_v[n] = (r.T for r in res)
        else:
            delta[n], new_m[n], new_v[n] = _adamw(wts[n], grads[n], mom[n], var[n], name="adamw_" + n)

    def lead(a, n):
        return a.reshape((1,) + wts[n].shape)

    return (loss, grad_x.reshape(nb, seq, dm), *[lead(grads[n], n) for n in WEIGHTS], *[lead(delta[n], n) for n in WEIGHTS],
            *[lead(new_m[n], n) for n in WEIGHTS], *[lead(new_v[n], n) for n in WEIGHTS])
```

```python
import functools
import math

import jax
import jax.numpy as jnp
from jax import lax
from jax.experimental import pallas as pl
from jax.experimental.pallas import tpu as pltpu

F32 = jnp.float32
MX = jnp.bfloat16

D = 1024
DFF = 2816
NH = 8
POOL_W = 512
POOL_G = 4
QL = 384
KVL = 256
ROPE = 32
NOPE = 64
LANE = 128
SUBLANE = 8
EPS = 1e-6
ATTN_SCALE = 1.0 / math.sqrt(96.0)
NEG = -1e30

Z_UP, Z_QL, Z_KV, Z_KR, Z_GP, Z_GM, Z_W = 0, 512, 896, 1152, 1280, 2304, 3328
KR_LANE = 64
LAT_W = 1280

ADAM_LR, ADAM_B1, ADAM_B2, ADAM_EPS, ADAM_WD, ADAM_STEP = 0.001, 0.9, 0.999, 1e-08, 0.01, 10

VMEM_LIMIT = 48 * 1024 * 1024
MESH = pl.DeviceIdType.MESH
N_DEV = 8
N_CHIP = 4


class _Comm:
    def __init__(self, ins, out_shapes, sems, start, finish, aliases=None):
        self.ins, self.out_shapes, self.sems = list(ins), list(out_shapes), list(sems)
        self.start, self.finish = start, finish
        self.aliases = aliases or {}


def _pcall(body, *, name, out_shape, grid=(), in_specs=None, out_specs=None, scratch=(), grid_spec=None, aliases=None,
           comm=None):
    params = pltpu.CompilerParams(vmem_limit_bytes=VMEM_LIMIT)
    kw = dict(name=name, compiler_params=params)
    if comm is None:
        if aliases:
            kw["input_output_aliases"] = aliases
        if grid_spec is not None:
            return pl.pallas_call(body, grid_spec=grid_spec, out_shape=out_shape, **kw)
        return pl.pallas_call(body, grid=grid, in_specs=in_specs, out_specs=out_specs, scratch_shapes=scratch,
                              out_shape=out_shape, **kw)
    single = not isinstance(out_shape, (list, tuple))
    outs = [out_shape] if single else list(out_shape)
    ospecs = [out_specs] if single else list(out_specs)
    n_in, n_out, n_ci, n_co, n_scr = len(in_specs), len(outs), len(comm.ins), len(comm.out_shapes), len(scratch)
    io = dict(aliases or {})
    io.update({n_in + i: n_out + o for i, o in comm.aliases.items()})

    def riding(*refs):
        ins, cins = refs[:n_in], refs[n_in:n_in + n_ci]
        at = n_in + n_ci
        os_, couts = refs[at:at + n_out], refs[at + n_out:at + n_out + n_co]
        at += n_out + n_co
        scr, csems = refs[at:at + n_scr], refs[at + n_scr:]
        if grid:
            first = functools.reduce(jnp.logical_and, [pl.program_id(d) == 0 for d in range(len(grid))])
            last = functools.reduce(jnp.logical_and, [pl.program_id(d) == grid[d] - 1 for d in range(len(grid))])
            pl.when(first)(lambda: comm.start(cins, couts, csems))
            body(*ins, *os_, *scr)
            pl.when(last)(lambda: comm.finish(cins, couts, csems))
        else:
            comm.start(cins, couts, csems)
            body(*ins, *os_, *scr)
            comm.finish(cins, couts, csems)

    call = pl.pallas_call(riding, grid=grid, in_specs=list(in_specs) + [HBM_SPEC] * n_ci, out_specs=ospecs + [HBM_SPEC] * n_co,
                          out_shape=outs + comm.out_shapes, scratch_shapes=list(scratch) + comm.sems,
                          input_output_aliases=io, **kw)

    def run(*args):
        res = call(*args, *comm.ins)
        main = list(res[:n_out])
        return (main[0] if single else main), list(res[n_out:])

    return run


def _pick(dim, target):
    best = None
    for t in range(LANE, min(dim, target) + 1, LANE):
        if dim % t == 0:
            best = t
    return dim if best is None else best


def _sds(shape, dtype):
    return jax.ShapeDtypeStruct(shape, dtype)


def _dw(a, g, *, name, tn=1024, comm=None):
    return _mm(a, g, mode="tn", out_dtype=F32, name=name, tm=256, tn=tn, tk=a.shape[0], n_outer=True, comm=comm)


def _mm(a, b, *, mode, out_dtype, name, tm=1024, tn=1024, tk=4096, n_outer=False, comm=None):
    if mode == "nn":
        (M, K), (K2, N) = a.shape, b.shape
    elif mode == "nt":
        (M, K), (N, K2) = a.shape, b.shape
    else:
        (K, M), (K2, N) = a.shape, b.shape
    assert K == K2, (name, a.shape, b.shape)
    tm, tn, tk = _pick(M, tm), _pick(N, tn), _pick(K, tk)
    nk = K // tk
    if n_outer:
        ij = lambda g0, g1: (g1, g0)
        grid = (N // tn, M // tm, nk)
    else:
        ij = lambda g0, g1: (g0, g1)
        grid = (M // tm, N // tn, nk)
    if mode == "tn":
        a_spec = pl.BlockSpec((tk, tm), lambda g0, g1, k: (k, ij(g0, g1)[0]))
    else:
        a_spec = pl.BlockSpec((tm, tk), lambda g0, g1, k: (ij(g0, g1)[0], k))
    if mode == "nt":
        b_spec = pl.BlockSpec((tn, tk), lambda g0, g1, k: (ij(g0, g1)[1], k))
    else:
        b_spec = pl.BlockSpec((tk, tn), lambda g0, g1, k: (k, ij(g0, g1)[1]))
    o_spec = pl.BlockSpec((tm, tn), lambda g0, g1, k: ij(g0, g1))
    dn = {"nn": (((1,), (0,)), ((), ())), "nt": (((1,), (1,)), ((), ())), "tn": (((0,), (0,)), ((), ()))}[mode]

    def dot(a_ref, b_ref):
        return lax.dot_general(a_ref[...].astype(MX), b_ref[...].astype(MX), dn, preferred_element_type=F32)

    def body_one(a_ref, b_ref, o_ref):
        o_ref[...] = dot(a_ref, b_ref).astype(o_ref.dtype)

    def body_acc(a_ref, b_ref, o_ref, acc_ref):
        k = pl.program_id(2)
        part = dot(a_ref, b_ref)

        @pl.when(k == 0)
        def _():
            acc_ref[...] = part

        @pl.when(k > 0)
        def _():
            acc_ref[...] += part

        @pl.when(k == nk - 1)
        def _():
            o_ref[...] = acc_ref[...].astype(o_ref.dtype)

    return _pcall(body_one if nk == 1 else body_acc, name=name, out_shape=_sds((M, N), out_dtype), grid=grid,
                  in_specs=[a_spec, b_spec], out_specs=o_spec, scratch=[] if nk == 1 else [pltpu.VMEM((tm, tn), F32)],
                  comm=comm)(a, b)


def _gu_shard(q):
    return (q % 2) * 2 + q // 2


def _ffn_up(h, w_st, *, name, tm=512, comm=None):
    T, dm = h.shape
    hw = w_st.shape[2]
    tm = _pick(T, tm)

    def body(h_ref, wg_ref, wu_ref, gu_ref, a_ref):
        hv = h_ref[...]
        g = jnp.dot(hv, wg_ref[0], preferred_element_type=F32)
        u = jnp.dot(hv, wu_ref[0], preferred_element_type=F32)
        gu_ref[:, :hw] = g.astype(gu_ref.dtype)
        gu_ref[:, hw:] = u.astype(gu_ref.dtype)
        a_ref[...] = (g * _sigmoid(g) * u).astype(a_ref.dtype)

    return _pcall(body, name=name, grid=(T // tm, 2),
                  in_specs=[pl.BlockSpec((tm, dm), lambda i, j: (i, 0)), pl.BlockSpec((1, dm, hw), lambda i, j: (j, 0, 0)),
                            pl.BlockSpec((1, dm, hw), lambda i, j: (2 + j, 0, 0))],
                  out_specs=[pl.BlockSpec((tm, 2 * hw), lambda i, j: (i, j)), pl.BlockSpec((tm, hw), lambda i, j: (i, j))],
                  out_shape=[_sds((T, 4 * hw), MX), _sds((T, 2 * hw), MX)], comm=comm)(h, w_st, w_st)


def _ffn_up_normed(x, mod3, gain, w_st, *, sub, name, tm=512, comm=None):
    T, dm = x.shape
    hw = w_st.shape[2]
    tm = _pick(T, tm)
    tps = (T // mod3.shape[0]) // tm

    def body(x_ref, mod_ref, n_ref, wg_ref, wu_ref, h_ref, gu_ref, a_ref):
        @pl.when(pl.program_id(1) == 0)
        def _():
            xv = x_ref[...]
            xn = xv * _rsq(xv) * n_ref[...]
            h_ref[...] = (xn * (1.0 + mod_ref[0, 3 * sub + 1:3 * sub + 2, :]) + mod_ref[0, 3 * sub:3 * sub + 1, :]).astype(h_ref.dtype)

        hv = h_ref[...]
        g = jnp.dot(hv, wg_ref[0], preferred_element_type=F32)
        u = jnp.dot(hv, wu_ref[0], preferred_element_type=F32)
        gu_ref[:, :hw] = g.astype(gu_ref.dtype)
        gu_ref[:, hw:] = u.astype(gu_ref.dtype)
        a_ref[...] = (g * _sigmoid(g) * u).astype(a_ref.dtype)

    return _pcall(body, name=name, grid=(T // tm, 2),
                  in_specs=[pl.BlockSpec((tm, dm), lambda i, j: (i, 0)), pl.BlockSpec((1, 9, dm), lambda i, j: (i // tps, 0, 0)),
                            pl.BlockSpec((1, dm), lambda i, j: (0, 0)), pl.BlockSpec((1, dm, hw), lambda i, j: (j, 0, 0)),
                            pl.BlockSpec((1, dm, hw), lambda i, j: (2 + j, 0, 0))],
                  out_specs=[pl.BlockSpec((tm, dm), lambda i, j: (i, 0)), pl.BlockSpec((tm, 2 * hw), lambda i, j: (i, j)),
                             pl.BlockSpec((tm, hw), lambda i, j: (i, j))],
                  out_shape=[_sds((T, dm), MX), _sds((T, 4 * hw), MX), _sds((T, 2 * hw), MX)],
                  comm=comm)(x, mod3, gain, w_st, w_st)


def _ffn_dact(df, gu, w_out, *, name, tm=512, comm=None):
    T, dm = df.shape
    hw = gu.shape[1] // 4
    tm = _pick(T, tm)

    def body(df_ref, gu_ref, wo_ref, dgu_ref):
        da = lax.dot_general(df_ref[...], wo_ref[...], (((1,), (1,)), ((), ())), preferred_element_type=F32)
        g = gu_ref[:, :hw].astype(F32)
        u = gu_ref[:, hw:].astype(F32)
        s = _sigmoid(g)
        dgu_ref[:, :hw] = (da * u * (s * (1.0 + g * (1.0 - s)))).astype(dgu_ref.dtype)
        dgu_ref[:, hw:] = (da * (g * s)).astype(dgu_ref.dtype)

    return _pcall(body, name=name, grid=(T // tm, 2),
                  in_specs=[pl.BlockSpec((tm, dm), lambda i, j: (i, 0)), pl.BlockSpec((tm, 2 * hw), lambda i, j: (i, j)),
                            pl.BlockSpec((hw, dm), lambda i, j: (j, 0))],
                  out_specs=pl.BlockSpec((tm, 2 * hw), lambda i, j: (i, j)), out_shape=_sds(gu.shape, MX),
                  comm=comm)(df, gu, w_out)


def _ffn_dh(dgu, w_st, *, name, tm=1024, comm=None):
    T = dgu.shape[0]
    _, dm, hw = w_st.shape
    tm = _pick(T, tm)

    def body(d_ref, w_ref, o_ref, acc_ref):
        q = pl.program_id(1)
        part = lax.dot_general(d_ref[...], w_ref[0], (((1,), (1,)), ((), ())), preferred_element_type=F32)

        @pl.when(q == 0)
        def _():
            acc_ref[...] = part

        @pl.when(q > 0)
        def _():
            acc_ref[...] += part

        @pl.when(q == 3)
        def _():
            o_ref[...] = acc_ref[...]

    return _pcall(body, name=name, grid=(T // tm, 4),
                  in_specs=[pl.BlockSpec((tm, hw), lambda i, q: (i, q)), pl.BlockSpec((1, dm, hw), lambda i, q: (_gu_shard(q), 0, 0))],
                  out_specs=pl.BlockSpec((tm, dm), lambda i, q: (i, 0)), out_shape=_sds((T, dm), F32),
                  scratch=[pltpu.VMEM((tm, dm), F32)], comm=comm)(dgu, w_st)


def _ffn_dw_in(h, dgu, *, name, rows=None, tm=256, comm=None):
    T, dm = h.shape
    hw = dgu.shape[1] // 4
    first, count = rows if rows is not None else (0, dm)
    tm = _pick(count, tm)
    skip = first // tm

    def body(h_ref, d_ref, o_ref):
        o_ref[0] = lax.dot_general(h_ref[...], d_ref[...], (((0,), (0,)), ((), ())), preferred_element_type=F32)

    return _pcall(body, name=name, grid=(4, count // tm),
                  in_specs=[pl.BlockSpec((T, tm), lambda q, i: (0, skip + i)), pl.BlockSpec((T, hw), lambda q, i: (0, q))],
                  out_specs=pl.BlockSpec((1, tm, hw), lambda q, i: (_gu_shard(q), i, 0)),
                  out_shape=_sds((4, count, hw), F32), comm=comm)(h, dgu)


def _rsq(x):
    return lax.rsqrt(jnp.mean(x * x, axis=-1, keepdims=True) + EPS)


def _sigmoid(x):
    return 1.0 / (1.0 + jnp.exp(-x))


def _row_spec(tm, w):
    return pl.BlockSpec((tm, w), lambda i: (i, 0))


def _fwd_block(x_prev, f_prev, mod3, gain, *, sub, coef, name, tm=256):
    T, dm = x_prev.shape
    tps = (T // mod3.shape[0]) // tm
    has_f = f_prev is not None
    mod_spec = pl.BlockSpec((1, 9, dm), lambda i: (i // tps, 0, 0))
    vec_spec = pl.BlockSpec((1, dm), lambda i: (0, 0))

    def body(*refs):
        if has_f:
            x_ref, f_ref, mod_ref, n_ref, xo_ref, h_ref = refs
            x = x_ref[...] + coef * mod_ref[0, 3 * sub - 1:3 * sub, :] * f_ref[...]
            xo_ref[...] = x
        else:
            x_ref, mod_ref, n_ref, h_ref = refs
            x = x_ref[...]
        xn = x * _rsq(x) * n_ref[...]
        h = xn * (1.0 + mod_ref[0, 3 * sub + 1:3 * sub + 2, :]) + mod_ref[0, 3 * sub:3 * sub + 1, :]
        h_ref[...] = h.astype(h_ref.dtype)

    row = _row_spec(tm, dm)
    if has_f:
        return _pcall(body, name=name, grid=(T // tm,), in_specs=[row, row, mod_spec, vec_spec], out_specs=[row, row],
                      out_shape=[_sds((T, dm), F32), _sds((T, dm), MX)])(x_prev, f_prev, mod3, gain)
    return _pcall(body, name=name, grid=(T // tm,), in_specs=[row, mod_spec, vec_spec], out_specs=row,
                  out_shape=_sds((T, dm), MX))(x_prev, mod3, gain)


def _mm_resid_norm(a, w, x_prev, mod3, gain, *, sub, coef, name, tm=512, comm=None):
    T, k = a.shape
    dm = w.shape[1]
    tm = _pick(T, tm)
    tps = (T // mod3.shape[0]) // tm

    def body(a_ref, w_ref, x_ref, mod_ref, n_ref, f_ref, xo_ref, h_ref):
        f = jnp.dot(a_ref[...], w_ref[...], preferred_element_type=F32)
        f_ref[...] = f
        x = x_ref[...] + coef * mod_ref[0, 3 * sub - 1:3 * sub, :] * f
        xo_ref[...] = x
        xn = x * _rsq(x) * n_ref[...]
        h_ref[...] = (xn * (1.0 + mod_ref[0, 3 * sub + 1:3 * sub + 2, :]) + mod_ref[0, 3 * sub:3 * sub + 1, :]).astype(h_ref.dtype)

    row = _row_spec(tm, dm)
    return _pcall(body, name=name, grid=(T // tm,),
                  in_specs=[_row_spec(tm, k), pl.BlockSpec((k, dm), lambda i: (0, 0)), row,
                            pl.BlockSpec((1, 9, dm), lambda i: (i // tps, 0, 0)), pl.BlockSpec((1, dm), lambda i: (0, 0))],
                  out_specs=[row, row, row], out_shape=[_sds((T, dm), F32), _sds((T, dm), F32), _sds((T, dm), MX)],
                  comm=comm)(a, w, x_prev, mod3, gain)


def _mm_loss(a, w, x2, tgt, mod3, *, name, tm=512):
    T, k = a.shape
    dm = w.shape[1]
    tm = _pick(T, tm)
    tps = (T // mod3.shape[0]) // tm
    mod_spec = pl.BlockSpec((1, 9, dm), lambda i: (i // tps, 0, 0))
    row = _row_spec(tm, dm)
    stat_spec = pl.BlockSpec((1, SUBLANE, dm), lambda i: (i // tps, 0, 0))
    loss_spec = pl.BlockSpec((SUBLANE, LANE), lambda i: (0, 0))

    def body(a_ref, w_ref, x_ref, t_ref, mod_ref, dy_ref, df_ref, st_ref, loss_ref):
        i = pl.program_id(0)
        g = mod_ref[0, 8:9, :]
        f = jnp.dot(a_ref[...], w_ref[...], preferred_element_type=F32)
        err = x_ref[...] + 0.5 * g * f - t_ref[...]
        dy = err * (1.0 / dm)
        dy_ref[...] = dy
        df_ref[...] = (0.5 * g * dy).astype(df_ref.dtype)
        dgate = jnp.sum(0.5 * dy * f, axis=0, keepdims=True)
        part = 0.5 * jnp.sum(jnp.sum(err * err, axis=0, keepdims=True), axis=1, keepdims=True) * (1.0 / dm)

        @pl.when(i % tps == 0)
        def _():
            st_ref[...] = jnp.zeros_like(st_ref)

        @pl.when(i == 0)
        def _():
            loss_ref[...] = jnp.zeros_like(loss_ref)

        st_ref[0, 0:1, :] += dgate
        loss_ref[...] += jnp.broadcast_to(part, loss_ref.shape)

    return _pcall(body, name=name, grid=(T // tm,),
                  in_specs=[_row_spec(tm, k), pl.BlockSpec((k, dm), lambda i: (0, 0)), row, row, mod_spec],
                  out_specs=[row, row, stat_spec, loss_spec],
                  out_shape=[_sds((T, dm), F32), _sds((T, dm), MX), _sds((mod3.shape[0], SUBLANE, dm), F32),
                             _sds((SUBLANE, LANE), F32)])(a, w, x2, tgt, mod3)


def _norm_bwd_tail(dhv, x_ref, dxi_ref, f_ref, mod_ref, n_ref, dx_ref, df_ref, st_ref, *, first, sub, coef):
    x = x_ref[...]
    r = _rsq(x)
    xhat = x * r
    n = n_ref[...]
    d_shift = jnp.sum(dhv, axis=0, keepdims=True)
    d_scale = jnp.sum(dhv * (xhat * n), axis=0, keepdims=True)
    dxn = dhv * (1.0 + mod_ref[0, 3 * sub + 1:3 * sub + 2, :])
    d_gain = jnp.sum(dxn * xhat, axis=0, keepdims=True)
    dxhat = dxn * n
    dx = dxi_ref[...] + r * (dxhat - xhat * jnp.mean(dxhat * xhat, axis=-1, keepdims=True))
    dx_ref[...] = dx

    @pl.when(first)
    def _():
        st_ref[...] = jnp.zeros_like(st_ref)

    st_ref[0, 0:1, :] += d_shift
    st_ref[0, 1:2, :] += d_scale
    st_ref[0, 2:3, :] += d_gain
    if f_ref is not None:
        st_ref[0, 3:4, :] += jnp.sum(coef * dx * f_ref[...], axis=0, keepdims=True)
        df_ref[...] = (coef * mod_ref[0, 3 * sub - 1:3 * sub, :] * dx).astype(df_ref.dtype)


def _ffn_dh_norm(dgu, w_st, x_cur, dx_in, f_prev, mod3, gain, *, sub, coef, name, tm=512, comm=None):
    T = dgu.shape[0]
    _, dm, hw = w_st.shape
    tm = _pick(T, tm)
    nb = mod3.shape[0]
    tps = (T // nb) // tm

    def body(d_ref, w_ref, x_ref, dxi_ref, f_ref, mod_ref, n_ref, dx_ref, df_ref, st_ref, acc_ref):
        i = pl.program_id(0)
        q = pl.program_id(1)
        part = lax.dot_general(d_ref[...], w_ref[0], (((1,), (1,)), ((), ())), preferred_element_type=F32)

        @pl.when(q == 0)
        def _():
            acc_ref[...] = part

        @pl.when(q > 0)
        def _():
            acc_ref[...] += part

        @pl.when(q == 3)
        def _():
            _norm_bwd_tail(acc_ref[...], x_ref, dxi_ref, f_ref, mod_ref, n_ref, dx_ref, df_ref, st_ref,
                           first=i % tps == 0, sub=sub, coef=coef)

    row = pl.BlockSpec((tm, dm), lambda i, q: (i, 0))
    return _pcall(body, name=name, grid=(T // tm, 4),
                  in_specs=[pl.BlockSpec((tm, hw), lambda i, q: (i, q)), pl.BlockSpec((1, dm, hw), lambda i, q: (_gu_shard(q), 0, 0)),
                            row, row, row, pl.BlockSpec((1, 9, dm), lambda i, q: (i // tps, 0, 0)),
                            pl.BlockSpec((1, dm), lambda i, q: (0, 0))],
                  out_specs=[row, row, pl.BlockSpec((1, SUBLANE, dm), lambda i, q: (i // tps, 0, 0))],
                  out_shape=[_sds((T, dm), F32), _sds((T, dm), MX), _sds((nb, SUBLANE, dm), F32)],
                  scratch=[pltpu.VMEM((tm, dm), F32)], comm=comm)(dgu, w_st, x_cur, dx_in, f_prev, mod3, gain)


def _mm_nt_norm(a, b, x_cur, dx_in, f_prev, mod3, gain, *, sub, coef, name, tm=512, comm=None):
    T, k = a.shape
    dm = b.shape[0]
    tm = _pick(T, tm)
    nb = mod3.shape[0]
    tps = (T // nb) // tm

    def body(a_ref, b_ref, x_ref, dxi_ref, f_ref, mod_ref, n_ref, dx_ref, df_ref, st_ref):
        dh = lax.dot_general(a_ref[...], b_ref[...], (((1,), (1,)), ((), ())), preferred_element_type=F32)
        _norm_bwd_tail(dh, x_ref, dxi_ref, f_ref, mod_ref, n_ref, dx_ref, df_ref, st_ref,
                       first=pl.program_id(0) % tps == 0, sub=sub, coef=coef)

    row = _row_spec(tm, dm)
    return _pcall(body, name=name, grid=(T // tm,),
                  in_specs=[_row_spec(tm, k), pl.BlockSpec((dm, k), lambda i: (0, 0)), row, row, row,
                            pl.BlockSpec((1, 9, dm), lambda i: (i // tps, 0, 0)), pl.BlockSpec((1, dm), lambda i: (0, 0))],
                  out_specs=[row, row, pl.BlockSpec((1, SUBLANE, dm), lambda i: (i // tps, 0, 0))],
                  out_shape=[_sds((T, dm), F32), _sds((T, dm), MX), _sds((nb, SUBLANE, dm), F32)],
                  comm=comm)(a, b, x_cur, dx_in, f_prev, mod3, gain)


def _bwd_block(x_cur, dh, dx_in, f_prev, mod3, gain, *, sub, coef, name, tm=256, comm=None):
    T, dm = x_cur.shape
    nb = mod3.shape[0]
    tps = (T // nb) // tm
    has_f = f_prev is not None
    mod_spec = pl.BlockSpec((1, 9, dm), lambda i: (i // tps, 0, 0))
    vec_spec = pl.BlockSpec((1, dm), lambda i: (0, 0))
    stat_spec = pl.BlockSpec((1, SUBLANE, dm), lambda i: (i // tps, 0, 0))
    row = _row_spec(tm, dm)

    def body(*refs):
        if has_f:
            x_ref, dh_ref, dxi_ref, f_ref, mod_ref, n_ref, dx_ref, df_ref, st_ref = refs
        else:
            x_ref, dh_ref, dxi_ref, mod_ref, n_ref, dx_ref, st_ref = refs
            f_ref = df_ref = None
        _norm_bwd_tail(dh_ref[...], x_ref, dxi_ref, f_ref, mod_ref, n_ref, dx_ref, df_ref, st_ref,
                       first=pl.program_id(0) % tps == 0, sub=sub, coef=coef)

    st_shape = _sds((nb, SUBLANE, dm), F32)
    if has_f:
        return _pcall(body, name=name, grid=(T // tm,), in_specs=[row, row, row, row, mod_spec, vec_spec],
                      out_specs=[row, row, stat_spec],
                      out_shape=[_sds((T, dm), F32), _sds((T, dm), MX), st_shape], comm=comm)(x_cur, dh, dx_in, f_prev, mod3, gain)
    return _pcall(body, name=name, grid=(T // tm,), in_specs=[row, row, row, mod_spec, vec_spec],
                  out_specs=[row, stat_spec], out_shape=[_sds((T, dm), F32), st_shape], comm=comm)(x_cur, dh, dx_in, mod3, gain)


def _merge_fwd(z, br_pool, br_mla, *, name, tm=256):
    T = z.shape[0]

    def body(z_ref, bp_ref, bm_ref, o_ref):
        gp = z_ref[:, Z_GP:Z_GP + D]
        gm = z_ref[:, Z_GM:Z_GM + D]
        o_ref[...] = (_sigmoid(gp) * bp_ref[...] + _sigmoid(gm) * bm_ref[...]).astype(o_ref.dtype)

    return _pcall(body, name=name, grid=(T // tm,), in_specs=[_row_spec(tm, Z_W), _row_spec(tm, D), _row_spec(tm, D)],
                  out_specs=_row_spec(tm, D), out_shape=_sds((T, D), MX))(z, br_pool, br_mla)


def _merge_bwd(z, br_pool, br_mla, dmerged, *, name, tm=256):
    T = z.shape[0]

    def body(z_ref, bp_ref, bm_ref, dm_ref, dbp_ref, dbm_ref, dg_ref):
        dm = dm_ref[...]
        sp = _sigmoid(z_ref[:, Z_GP:Z_GP + D])
        sm = _sigmoid(z_ref[:, Z_GM:Z_GM + D])
        dbp_ref[...] = (dm * sp).astype(dbp_ref.dtype)
        dbm_ref[...] = (dm * sm).astype(dbm_ref.dtype)
        dg_ref[:, :D] = (dm * bp_ref[...] * sp * (1.0 - sp)).astype(dg_ref.dtype)
        dg_ref[:, D:] = (dm * bm_ref[...] * sm * (1.0 - sm)).astype(dg_ref.dtype)

    return _pcall(body, name=name, grid=(T // tm,),
                  in_specs=[_row_spec(tm, Z_W), _row_spec(tm, D), _row_spec(tm, D), _row_spec(tm, D)],
                  out_specs=[_row_spec(tm, D), _row_spec(tm, D), _row_spec(tm, 2 * D)],
                  out_shape=[_sds((T, D), MX), _sds((T, D), MX), _sds((T, 2 * D), MX)])(z, br_pool, br_mla, dmerged)


def _shift_down(x, k, row):
    return jnp.where(row >= k, pltpu.roll(x, k, 0), 0.0)


def _shift_up(x, k, row, n):
    return jnp.where(row < n - k, pltpu.roll(x, n - k, 0), 0.0)


def _pool_fwd(z, pool_grp, pool_scale, *, nb, name):
    T = z.shape[0]
    S = T // nb
    blk = pl.BlockSpec((S, LANE), lambda b, g: (b, g))

    def body(u_ref, w_ref, s_ref, pooled_ref, mixed_ref, scaled_ref):
        g = pl.program_id(1)
        u = u_ref[...]
        row = lax.broadcasted_iota(jnp.int32, u.shape, 0)
        s2 = u + _shift_down(u, 1, row)
        s4 = s2 + _shift_down(s2, 2, row)
        s8 = s4 + _shift_down(s4, 4, row)
        s16 = s8 + _shift_down(s8, 8, row)
        win = jnp.where(g == 0, s2, jnp.where(g == 1, s4, jnp.where(g == 2, s8, s16)))
        width = lax.shift_left(jnp.int32(2), g)
        cnt = jnp.minimum(row + 1, width).astype(F32)
        pooled = (win / cnt - u).astype(MX)
        pooled_ref[...] = pooled
        mixed = jnp.dot(pooled, w_ref[0], preferred_element_type=F32)
        mixed_ref[...] = mixed
        scaled_ref[...] = (mixed * s_ref[...]).astype(scaled_ref.dtype)

    return _pcall(body, name=name, grid=(nb, POOL_G),
                  in_specs=[blk, pl.BlockSpec((1, LANE, LANE), lambda b, g: (g, 0, 0)),
                            pl.BlockSpec((1, LANE), lambda b, g: (0, g))],
                  out_specs=[blk, blk, blk],
                  out_shape=[_sds((T, POOL_W), MX), _sds((T, POOL_W), F32), _sds((T, POOL_W), MX)])(z, pool_grp, pool_scale)


def _pool_bwd(dscaled, mixed, pooled, pool_grp, pool_scale, *, nb, name):
    T = dscaled.shape[0]
    S = T // nb
    blk = pl.BlockSpec((S, LANE), lambda g, b: (b, g))

    def body(ds_ref, mixed_ref, pooled_ref, w_ref, s_ref, du_ref, dw_ref, dsc_ref):
        g = pl.program_id(0)
        b = pl.program_id(1)
        ds = ds_ref[...]
        dsc_ref[0] = jnp.sum(ds * mixed_ref[...], axis=0, keepdims=True)
        dmixed = (ds * s_ref[...]).astype(MX)
        dw = lax.dot_general(pooled_ref[...], dmixed, (((0,), (0,)), ((), ())), preferred_element_type=F32)

        @pl.when(b == 0)
        def _():
            dw_ref[0] = dw

        @pl.when(b > 0)
        def _():
            dw_ref[0] += dw
        dpooled = lax.dot_general(dmixed, w_ref[0], (((1,), (1,)), ((), ())), preferred_element_type=F32)
        row = lax.broadcasted_iota(jnp.int32, dpooled.shape, 0)
        width = lax.shift_left(jnp.int32(2), g)
        q = dpooled / jnp.minimum(row + 1, width).astype(F32)
        r2 = q + _shift_up(q, 1, row, S)
        r4 = r2 + _shift_up(r2, 2, row, S)
        r8 = r4 + _shift_up(r4, 4, row, S)
        r16 = r8 + _shift_up(r8, 8, row, S)
        win = jnp.where(g == 0, r2, jnp.where(g == 1, r4, jnp.where(g == 2, r8, r16)))
        du_ref[...] = (win - dpooled).astype(du_ref.dtype)

    return _pcall(body, name=name, grid=(POOL_G, nb),
                  in_specs=[blk, blk, blk, pl.BlockSpec((1, LANE, LANE), lambda g, b: (g, 0, 0)),
                            pl.BlockSpec((1, LANE), lambda g, b: (0, g))],
                  out_specs=[blk, pl.BlockSpec((1, LANE, LANE), lambda g, b: (g, 0, 0)),
                             pl.BlockSpec((1, 1, LANE), lambda g, b: (b, 0, g))],
                  out_shape=[_sds((T, POOL_W), MX), _sds((POOL_G, LANE, LANE), F32), _sds((nb, 1, POOL_W), F32)],
                  )(dscaled, mixed, pooled, pool_grp, pool_scale)


def _lat_fwd(z, q_gain, kv_gain, *, name, tm=256):
    T = z.shape[0]

    def body(z_ref, qg_ref, kg_ref, qn_ref, kvn_ref):
        ql = z_ref[:, Z_QL:Z_QL + QL]
        kv = z_ref[:, Z_KV:Z_KV + KVL]
        qn_ref[...] = (ql * _rsq(ql) * qg_ref[...]).astype(qn_ref.dtype)
        kvn_ref[...] = (kv * _rsq(kv) * kg_ref[...]).astype(kvn_ref.dtype)

    return _pcall(body, name=name, grid=(T // tm,),
                  in_specs=[_row_spec(tm, LAT_W), pl.BlockSpec((1, QL), lambda i: (0, 0)), pl.BlockSpec((1, KVL), lambda i: (0, 0))],
                  out_specs=[_row_spec(tm, QL), _row_spec(tm, KVL)],
                  out_shape=[_sds((T, QL), MX), _sds((T, KVL), MX)])(z, q_gain, kv_gain)


def _lat_bwd(z, dqn, dkvn, q_gain, kv_gain, *, nb, name, tm=256):
    T = z.shape[0]
    tps = (T // nb) // tm

    def norm_bwd(x, dy, gain):
        r = _rsq(x)
        xhat = x * r
        dgain = jnp.sum(dy * xhat, axis=0, keepdims=True)
        dxhat = dy * gain
        return r * (dxhat - xhat * jnp.mean(dxhat * xhat, axis=-1, keepdims=True)), dgain

    def body(z_ref, dq_ref, dkv_ref, qg_ref, kg_ref, dql_ref, dkvl_ref, sq_ref, sk_ref):
        i = pl.program_id(0)
        dql, dqg = norm_bwd(z_ref[:, Z_QL:Z_QL + QL], dq_ref[...], qg_ref[...])
        dkvl, dkg = norm_bwd(z_ref[:, Z_KV:Z_KV + KVL], dkv_ref[...], kg_ref[...])
        dql_ref[...] = dql.astype(dql_ref.dtype)
        dkvl_ref[...] = dkvl.astype(dkvl_ref.dtype)

        @pl.when(i % tps == 0)
        def _():
            sq_ref[...] = jnp.zeros_like(sq_ref)
            sk_ref[...] = jnp.zeros_like(sk_ref)

        sq_ref[0, 0:1, :] += dqg
        sk_ref[0, 0:1, :] += dkg

    return _pcall(body, name=name, grid=(T // tm,),
                  in_specs=[_row_spec(tm, LAT_W), _row_spec(tm, QL), _row_spec(tm, KVL),
                            pl.BlockSpec((1, QL), lambda i: (0, 0)), pl.BlockSpec((1, KVL), lambda i: (0, 0))],
                  out_specs=[_row_spec(tm, QL), _row_spec(tm, KVL),
                             pl.BlockSpec((1, SUBLANE, QL), lambda i: (i // tps, 0, 0)),
                             pl.BlockSpec((1, SUBLANE, KVL), lambda i: (i // tps, 0, 0))],
                  out_shape=[_sds((T, QL), MX), _sds((T, KVL), MX), _sds((nb, SUBLANE, QL), F32), _sds((nb, SUBLANE, KVL), F32)],
                  )(z, dqn, dkvn, q_gain, kv_gain)


def _lane_masks(shape):
    lane = lax.broadcasted_iota(jnp.int32, shape, len(shape) - 1)
    m_n = lane < NOPE
    m_r = jnp.logical_and(lane >= KR_LANE, lane < KR_LANE + ROPE)
    first_half = lane < KR_LANE + ROPE // 2
    return m_n, m_r, first_half


def _rot(y, first_half):
    return jnp.where(first_half, -pltpu.roll(y, LANE - ROPE // 2, 1), pltpu.roll(y, ROPE // 2, 1))


def _rot_t(v, first_half, m_r):
    return jnp.where(m_r, jnp.where(first_half, pltpu.roll(v, LANE - ROPE // 2, 1), -pltpu.roll(v, ROPE // 2, 1)), 0.0)


def _prep_fwd(qp, kvp, z, cos, sin, q_gain, k_gain, *, name, tm=256):
    T = qp.shape[0]
    slab = pl.BlockSpec((tm, LANE), lambda i: (i, 0))
    kr_spec = pl.BlockSpec((tm, LANE), lambda i: (i, Z_KR // LANE))
    vec = pl.BlockSpec((1, LANE), lambda i: (0, 0))

    def body(qp_ref, kvp_ref, kr_ref, cos_ref, sin_ref, qg_ref, kg_ref, q_ref, k_ref, v_ref):
        m_n, m_r, first_half = _lane_masks((tm, LANE))
        c = cos_ref[...]
        s = sin_ref[...]
        qg = qg_ref[...]
        kg = kg_ref[...]
        xr = kr_ref[...]
        rr = lax.rsqrt(jnp.sum(xr * xr, axis=-1, keepdims=True) * (1.0 / ROPE) + EPS)
        yr = xr * rr * kg
        kr = jnp.where(m_r, yr * c + _rot(yr, first_half) * s, 0.0)
        for h in range(NH):
            x = qp_ref[:, h * LANE:(h + 1) * LANE]
            x2 = x * x
            rn = lax.rsqrt(jnp.sum(jnp.where(m_n, x2, 0.0), axis=-1, keepdims=True) * (1.0 / NOPE) + EPS)
            rq = lax.rsqrt(jnp.sum(jnp.where(m_r, x2, 0.0), axis=-1, keepdims=True) * (1.0 / ROPE) + EPS)
            y = x * jnp.where(m_n, rn, jnp.where(m_r, rq, 0.0)) * qg
            q_ref[:, h * LANE:(h + 1) * LANE] = (y * c + _rot(y, first_half) * s).astype(q_ref.dtype)
            xk = kvp_ref[:, h * LANE:(h + 1) * LANE]
            rk = lax.rsqrt(jnp.sum(jnp.where(m_n, xk * xk, 0.0), axis=-1, keepdims=True) * (1.0 / NOPE) + EPS)
            k_ref[:, h * LANE:(h + 1) * LANE] = (jnp.where(m_n, xk * rk * kg, 0.0) + kr).astype(k_ref.dtype)
        v_ref[...] = kvp_ref[:, NH * LANE:].astype(v_ref.dtype)

    return _pcall(body, name=name, grid=(T // tm,),
                  in_specs=[_row_spec(tm, NH * LANE), _row_spec(tm, NH * LANE + NH * NOPE), kr_spec, slab, slab, vec, vec],
                  out_specs=[_row_spec(tm, NH * LANE), _row_spec(tm, NH * LANE), _row_spec(tm, NH * NOPE)],
                  out_shape=[_sds((T, NH * LANE), MX), _sds((T, NH * LANE), MX), _sds((T, NH * NOPE), MX)],
                  )(qp, kvp, z, cos, sin, q_gain, k_gain)


def _prep_bwd(dq, dk, dv, qp, kvp, z, cos, sin, q_gain, k_gain, *, nb, name, tm=256):
    T = qp.shape[0]
    tps = (T // nb) // tm
    slab = pl.BlockSpec((tm, LANE), lambda i: (i, 0))
    kr_spec = pl.BlockSpec((tm, LANE), lambda i: (i, Z_KR // LANE))
    vec = pl.BlockSpec((1, LANE), lambda i: (0, 0))

    def body(dq_ref, dk_ref, dv_ref, qp_ref, kvp_ref, kr_ref, cos_ref, sin_ref, qg_ref, kg_ref,
             dqp_ref, dkvp_ref, dkr_ref, st_ref):
        i = pl.program_id(0)
        m_n, m_r, first_half = _lane_masks((tm, LANE))
        c = cos_ref[...]
        s = sin_ref[...]
        qg = qg_ref[...]
        kg = kg_ref[...]
        dqg = jnp.zeros((1, LANE), F32)
        dkg = jnp.zeros((1, LANE), F32)
        dkr_sum = jnp.zeros((tm, LANE), F32)
        for h in range(NH):
            x = qp_ref[:, h * LANE:(h + 1) * LANE]
            x2 = x * x
            rn = lax.rsqrt(jnp.sum(jnp.where(m_n, x2, 0.0), axis=-1, keepdims=True) * (1.0 / NOPE) + EPS)
            rq = lax.rsqrt(jnp.sum(jnp.where(m_r, x2, 0.0), axis=-1, keepdims=True) * (1.0 / ROPE) + EPS)
            rfac = jnp.where(m_n, rn, jnp.where(m_r, rq, 0.0))
            xhat = x * rfac
            do = dq_ref[:, h * LANE:(h + 1) * LANE]
            dy = do * c + _rot_t(do * s, first_half, m_r)
            dqg = dqg + jnp.sum(dy * xhat, axis=0, keepdims=True)
            dxhat = dy * qg
            t = dxhat * xhat
            mean_n = jnp.sum(jnp.where(m_n, t, 0.0), axis=-1, keepdims=True) * (1.0 / NOPE)
            mean_r = jnp.sum(jnp.where(m_r, t, 0.0), axis=-1, keepdims=True) * (1.0 / ROPE)
            dqp_ref[:, h * LANE:(h + 1) * LANE] = (
                rfac * (dxhat - xhat * jnp.where(m_n, mean_n, jnp.where(m_r, mean_r, 0.0)))).astype(dqp_ref.dtype)

            xk = kvp_ref[:, h * LANE:(h + 1) * LANE]
            rk = lax.rsqrt(jnp.sum(jnp.where(m_n, xk * xk, 0.0), axis=-1, keepdims=True) * (1.0 / NOPE) + EPS)
            khat = jnp.where(m_n, xk * rk, 0.0)
            dko = dk_ref[:, h * LANE:(h + 1) * LANE]
            dkn = jnp.where(m_n, dko, 0.0)
            dkg = dkg + jnp.sum(dkn * khat, axis=0, keepdims=True)
            dkhat = dkn * kg
            mean_k = jnp.sum(dkhat * khat, axis=-1, keepdims=True) * (1.0 / NOPE)
            dkvp_ref[:, h * LANE:(h + 1) * LANE] = jnp.where(m_n, rk * (dkhat - khat * mean_k), 0.0).astype(dkvp_ref.dtype)
            dkr_sum = dkr_sum + jnp.where(m_r, dko, 0.0)
        dkvp_ref[:, NH * LANE:] = dv_ref[...].astype(dkvp_ref.dtype)

        xr = kr_ref[...]
        rr = lax.rsqrt(jnp.sum(xr * xr, axis=-1, keepdims=True) * (1.0 / ROPE) + EPS)
        rhat = xr * rr
        dyr = dkr_sum * c + _rot_t(dkr_sum * s, first_half, m_r)
        dkg = dkg + jnp.sum(dyr * rhat, axis=0, keepdims=True)
        drhat = dyr * kg
        mean_kr = jnp.sum(drhat * rhat, axis=-1, keepdims=True) * (1.0 / ROPE)
        dkr_ref[...] = jnp.where(m_r, rr * (drhat - rhat * mean_kr), 0.0).astype(dkr_ref.dtype)

        @pl.when(i % tps == 0)
        def _():
            st_ref[...] = jnp.zeros_like(st_ref)

        st_ref[0, 0:1, :] += dqg
        st_ref[0, 1:2, :] += dkg

    return _pcall(body, name=name, grid=(T // tm,),
                  in_specs=[_row_spec(tm, NH * LANE), _row_spec(tm, NH * LANE), _row_spec(tm, NH * NOPE),
                            _row_spec(tm, NH * LANE), _row_spec(tm, NH * LANE + NH * NOPE), kr_spec, slab, slab, vec, vec],
                  out_specs=[_row_spec(tm, NH * LANE), _row_spec(tm, NH * LANE + NH * NOPE), slab,
                             pl.BlockSpec((1, SUBLANE, LANE), lambda i: (i // tps, 0, 0))],
                  out_shape=[_sds((T, NH * LANE), MX), _sds((T, NH * LANE + NH * NOPE), MX), _sds((T, LANE), MX),
                             _sds((nb, SUBLANE, LANE), F32)],
                  )(dq, dk, dv, qp, kvp, z, cos, sin, q_gain, k_gain)


def _lower_triangle(t):
    return lax.broadcasted_iota(jnp.int32, (t, t), 1) <= lax.broadcasted_iota(jnp.int32, (t, t), 0)


def _attn_fwd(q, k, v, *, nb, name, tq=512, comm=None):
    T = q.shape[0]
    S = T // nb
    tq = _pick(S, tq)
    nq = S // tq
    tk = tq
    npair = NH // 2

    def body(q_ref, k_ref, v_ref, o_ref, lse_ref):
        qi = pl.program_id(2)
        lane = lax.broadcasted_iota(jnp.int32, (tq, LANE), 1)
        qs = [q_ref[:, hh * LANE:(hh + 1) * LANE] for hh in range(2)]

        def block(j, carry, diagonal):
            k0 = pl.multiple_of(j * tk, tk)
            vb = v_ref[pl.ds(k0, tk), :]
            new = []
            for hh in range(2):
                m, l, acc = carry[hh]
                kb = k_ref[pl.ds(k0, tk), hh * LANE:(hh + 1) * LANE]
                s = lax.dot_general(qs[hh], kb, (((1,), (1,)), ((), ())), preferred_element_type=F32) * ATTN_SCALE
                if diagonal:
                    s = jnp.where(_lower_triangle(tq), s, NEG)
                m_new = jnp.maximum(m, jnp.max(s, axis=-1, keepdims=True))
                p = jnp.exp(s - m_new)
                alpha = jnp.exp(m - m_new)
                l = alpha * l + jnp.sum(p, axis=-1, keepdims=True)
                acc = alpha * acc + jnp.dot(p.astype(MX), vb, preferred_element_type=F32)
                new.append((m_new, l, acc))
            return tuple(new)

        init = tuple((jnp.full((tq, 1), NEG, F32), jnp.zeros((tq, 1), F32), jnp.zeros((tq, LANE), F32)) for _ in range(2))
        carry = lax.fori_loop(0, qi, lambda j, c: block(j, c, False), init)
        (m0, l0, acc0), (m1, l1, acc1) = block(qi, carry, True)
        o_ref[...] = jnp.where(lane < NOPE, acc0 / l0, acc1 / l1).astype(o_ref.dtype)
        lse_ref[...] = jnp.where(lane < NOPE, m0 + jnp.log(l0), m1 + jnp.log(l1))

    return _pcall(body, name=name, grid=(nb, npair, nq),
                  in_specs=[pl.BlockSpec((tq, 2 * LANE), lambda b, p, i: (b * nq + i, p)),
                            pl.BlockSpec((S, 2 * LANE), lambda b, p, i: (b, p)),
                            pl.BlockSpec((S, LANE), lambda b, p, i: (b, p))],
                  out_specs=[pl.BlockSpec((tq, LANE), lambda b, p, i: (b * nq + i, p)),
                             pl.BlockSpec((tq, LANE), lambda b, p, i: (b * nq + i, p))],
                  out_shape=[_sds((T, NH * NOPE), MX), _sds((T, NH * NOPE), F32)], comm=comm)(q, k, v)


def _attn_bwd(q, k, v, o, lse, do, *, nb, name, tq=512, comm=None):
    T = q.shape[0]
    S = T // nb
    tq = _pick(S, tq)
    nq = S // tq
    tk = tq
    npair = NH // 2

    def body(q_ref, k_ref, v_ref, o_ref, lse_ref, do_ref, dq_ref, dk_ref, dv_ref, delta_ref):
        lane = lax.broadcasted_iota(jnp.int32, (tq, LANE), 1)
        first = lane < NOPE
        dq_ref[...] = jnp.zeros_like(dq_ref)

        def delta_step(qi, _):
            q0 = pl.multiple_of(qi * tq, tq)
            prod = do_ref[pl.ds(q0, tq), :] * o_ref[pl.ds(q0, tq), :].astype(F32)
            d0 = jnp.sum(jnp.where(first, prod, 0.0), axis=-1, keepdims=True)
            d1 = jnp.sum(jnp.where(first, 0.0, prod), axis=-1, keepdims=True)
            delta_ref[pl.ds(q0, tq), :] = jnp.where(first, d0, d1)
            return 0

        lax.fori_loop(0, nq, delta_step, 0)

        def kv_step(kj, _):
            k0 = pl.multiple_of(kj * tk, tk)
            kbs = [k_ref[pl.ds(k0, tk), hh * LANE:(hh + 1) * LANE] for hh in range(2)]
            vb = v_ref[pl.ds(k0, tk), :]

            def q_block(qi, carry, diagonal):
                dk0, dk1, dv = carry
                dks = [dk0, dk1]
                q0 = pl.multiple_of(qi * tq, tq)
                dov = do_ref[pl.ds(q0, tq), :]
                lse_v = lse_ref[pl.ds(q0, tq), :]
                delta_v = delta_ref[pl.ds(q0, tq), :]
                for hh in range(2):
                    qh = q_ref[pl.ds(q0, tq), hh * LANE:(hh + 1) * LANE]
                    dob = jnp.where(first if hh == 0 else jnp.logical_not(first), dov, 0.0).astype(MX)
                    s = lax.dot_general(qh, kbs[hh], (((1,), (1,)), ((), ())), preferred_element_type=F32) * ATTN_SCALE
                    p = jnp.exp(s - lse_v[:, hh * NOPE:hh * NOPE + 1])
                    if diagonal:
                        p = jnp.where(_lower_triangle(tq), p, 0.0)
                    dp = lax.dot_general(dob, vb, (((1,), (1,)), ((), ())), preferred_element_type=F32)
                    ds = (p * (dp - delta_v[:, hh * NOPE:hh * NOPE + 1]) * ATTN_SCALE).astype(MX)
                    dks[hh] = dks[hh] + lax.dot_general(ds, qh, (((0,), (0,)), ((), ())), preferred_element_type=F32)
                    dv = dv + lax.dot_general(p.astype(MX), dob, (((0,), (0,)), ((), ())), preferred_element_type=F32)
                    dq_ref[pl.ds(q0, tq), hh * LANE:(hh + 1) * LANE] += jnp.dot(ds, kbs[hh], preferred_element_type=F32)
                return dks[0], dks[1], dv

            zero = jnp.zeros((tk, LANE), F32)
            carry = q_block(kj, (zero, zero, zero), True)
            dk0, dk1, dv = lax.fori_loop(kj + 1, nq, lambda qi, c: q_block(qi, c, False), carry)
            dk_ref[pl.ds(k0, tk), 0:LANE] = dk0
            dk_ref[pl.ds(k0, tk), LANE:2 * LANE] = dk1
            dv_ref[pl.ds(k0, tk), :] = dv
            return 0

        lax.fori_loop(0, nq, kv_step, 0)

    pair256 = pl.BlockSpec((S, 2 * LANE), lambda b, p: (b, p))
    pair128 = pl.BlockSpec((S, LANE), lambda b, p: (b, p))
    return _pcall(body, name=name, grid=(nb, npair),
                  in_specs=[pair256, pair256, pair128, pair128, pair128, pair128],
                  out_specs=[pair256, pair256, pair128],
                  out_shape=[_sds((T, NH * LANE), F32), _sds((T, NH * LANE), F32), _sds((T, NH * NOPE), F32)],
                  scratch=[pltpu.VMEM((S, LANE), F32)], comm=comm)(q, k, v, o, lse, do)


class _NoExchange:
    def __init__(self, weights):
        self.W, self.G = weights, {}

    def rider(self, name):
        return None

    def landed(self, name, outs):
        pass

    def grad(self, key, g):
        self.G[key] = g


def _run(plan, fn, *args, name, **kw):
    rider = plan.rider(name)
    if rider is None:
        return fn(*args, name=name, **kw)
    outs, landed = fn(*args, name=name, comm=rider, **kw)
    plan.landed(name, landed)
    return outs


def _layer_fwd_bwd(x, tgt, mod3, cos, sin, plan, P):
    nb = mod3.shape[0]
    W = plan.W
    h1, gu1, a1 = _run(plan, _ffn_up_normed, x, mod3, P["norm_ffn1"], W["ffn1_in"], sub=0, name="ffn1_up")
    f1, x1, h2 = _mm_resid_norm(a1, W["ffn1_out"], x, mod3, P["norm_mix"], sub=1, coef=0.5, name="ffn1_out")
    z = _run(plan, _mm, h2, W["w_in"], mode="nn", out_dtype=F32, name="mix_in", tn=1664)
    pooled, mixed, scaled = _pool_fwd(z, W["pool_grp"], P["pool_scale"], nb=nb, name="pool_fwd")
    br_pool = _mm(scaled, W["pool_proj"], mode="nn", out_dtype=F32, name="pool_proj")
    qn, kvn = _lat_fwd(z, P["q_a_norm"], P["kv_a_norm"], name="lat_fwd")
    qp = _mm(qn, W["q_up"], mode="nn", out_dtype=F32, name="q_up")
    kvp = _mm(kvn, W["kv_up"], mode="nn", out_dtype=F32, name="kv_up")
    q, k, v = _prep_fwd(qp, kvp, z, cos, sin, P["q_gain"], P["k_gain"], name="prep_fwd")
    attn, lse = _run(plan, _attn_fwd, q, k, v, nb=nb, name="attn_fwd")
    br_mla = _mm(attn, W["mla_proj"], mode="nn", out_dtype=F32, name="mla_proj")
    merged = _merge_fwd(z, br_pool, br_mla, name="merge_fwd")
    mo, x2, h3 = _mm_resid_norm(merged, W["w_out"], x1, mod3, P["norm_ffn2"], sub=2, coef=1.0, name="mix_out")
    gu2, a2 = _ffn_up(h3, W["ffn2_in"], name="ffn2_up")
    dy, df2, st_fin, loss = _mm_loss(a2, W["ffn2_out"], x2, tgt, mod3, name="ffn2_out_loss")

    plan.grad("ffn2_out", _dw(a2, df2, name="d_ffn2_out"))
    dgu2 = _ffn_dact(df2, gu2, W["ffn2_out"], name="ffn2_dact")
    plan.grad("ffn2_in", _ffn_dw_in(h3, dgu2, name="d_ffn2_in"))
    dx2, dmo, st3 = _run(plan, _ffn_dh_norm, dgu2, W["ffn2_in"], x2, dy, mo, mod3, P["norm_ffn2"], sub=2, coef=1.0,
                         name="d_ffn2_h")
    plan.grad("w_out", _dw(merged, dmo, name="d_mix_out"))
    dmerged = _mm(dmo, W["w_out"], mode="nt", out_dtype=F32, name="d_merged")
    dbr_pool, dbr_mla, dgates = _merge_bwd(z, br_pool, br_mla, dmerged, name="merge_bwd")
    plan.grad("pool_proj", _dw(scaled, dbr_pool, name="d_pool_proj"))
    dscaled = _mm(dbr_pool, W["pool_proj"], mode="nt", out_dtype=F32, name="d_pool_scaled")
    du_pool, d_pool_grp, d_pool_scale = _pool_bwd(dscaled, mixed, pooled, W["pool_grp"], P["pool_scale"], nb=nb, name="pool_bwd")
    plan.grad("mla_proj", _dw(attn, dbr_mla, name="d_mla_proj"))
    dattn = _mm(dbr_mla, W["mla_proj"], mode="nt", out_dtype=F32, name="d_attn")
    dq, dk, dv = _run(plan, _attn_bwd, q, k, v, attn, lse, dattn, nb=nb, name="attn_bwd")
    dqp, dkvp, dkr, st_prep = _prep_bwd(dq, dk, dv, qp, kvp, z, cos, sin, P["q_gain"], P["k_gain"], nb=nb, name="prep_bwd")
    plan.grad("q_up", _dw(qn, dqp, name="d_q_up"))
    dqn = _mm(dqp, W["q_up"], mode="nt", out_dtype=F32, name="d_qn")
    plan.grad("kv_up", _dw(kvn, dkvp, name="d_kv_up"))
    dkvn = _mm(dkvp, W["kv_up"], mode="nt", out_dtype=F32, name="d_kvn")
    dql, dkvl, st_q, st_kv = _lat_bwd(z, dqn, dkvn, P["q_a_norm"], P["kv_a_norm"], nb=nb, name="lat_bwd")
    dz = jnp.concatenate([du_pool, dql, dkvl, dkr, dgates], axis=1)
    plan.grad("w_in", _run(plan, _dw, h2, dz, name="d_mix_in", tn=1664))
    dx1, df1, st2 = _run(plan, _mm_nt_norm, dz, W["w_in"], x1, dx2, f1, mod3, P["norm_mix"], sub=1, coef=0.5, name="d_mix_h")
    plan.grad("ffn1_out", _run(plan, _dw, a1, df1, name="d_ffn1_out"))
    dgu1 = _run(plan, _ffn_dact, df1, gu1, W["ffn1_out"], name="ffn1_dact")
    half = x.shape[1] // 2
    plan.grad("ffn1_in@0", _run(plan, _ffn_dw_in, h1, dgu1, rows=(0, half), name="d_ffn1_in_a"))
    plan.grad("ffn1_in@1", _run(plan, _ffn_dw_in, h1, dgu1, rows=(half, half), name="d_ffn1_in_b"))
    dh1 = _run(plan, _ffn_dh, dgu1, W["ffn1_in"], name="d_ffn1_h")
    grad_x, st1 = _run(plan, _bwd_block, x, dh1, dx1, None, mod3, P["norm_ffn1"], sub=0, coef=0.0, name="bwd_norm1")

    return loss, grad_x, (st1, st2, st3, st_fin, st_q, st_kv, st_prep, d_pool_scale), d_pool_grp


def _w_in_to_kernel(w):
    k = w.shape[0]
    zeros = lambda n: jnp.zeros((k, n), w.dtype)
    return jnp.concatenate([w[:, 0:1152], zeros(KR_LANE), w[:, 1152:1184], zeros(LANE - KR_LANE - ROPE), w[:, 1184:]], axis=1)


def _w_in_from_kernel(g):
    return jnp.concatenate([g[:, 0:1152], g[:, Z_KR + KR_LANE:Z_KR + KR_LANE + ROPE], g[:, Z_GP:]], axis=1)


def _q_up_to_kernel(w):
    k = w.shape[0]
    return jnp.pad(w.reshape(k, NH, NOPE + ROPE), ((0, 0), (0, 0), (0, LANE - NOPE - ROPE))).reshape(k, NH * LANE)


def _q_up_from_kernel(g):
    k = g.shape[0]
    return g.reshape(k, NH, LANE)[:, :, :NOPE + ROPE].reshape(k, NH * (NOPE + ROPE))


def _kv_up_to_kernel(w):
    k = w.shape[0]
    w3 = w.reshape(k, NH, 2 * NOPE)
    kpart = jnp.pad(w3[:, :, :NOPE], ((0, 0), (0, 0), (0, LANE - NOPE))).reshape(k, NH * LANE)
    return jnp.concatenate([kpart, w3[:, :, NOPE:].reshape(k, NH * NOPE)], axis=1)


def _kv_up_from_kernel(g):
    k = g.shape[0]
    kpart = g[:, :NH * LANE].reshape(k, NH, LANE)[:, :, :NOPE]
    vpart = g[:, NH * LANE:].reshape(k, NH, NOPE)
    return jnp.concatenate([kpart, vpart], axis=2).reshape(k, NH * 2 * NOPE)


def _gain_slab(nope, rope):
    return jnp.concatenate([nope, rope, jnp.zeros((1, LANE - NOPE - ROPE), nope.dtype)], axis=1)


def _rope_tables(positions):
    inv_freq = 10000.0 ** (-jnp.arange(0, ROPE, 2, dtype=F32) / ROPE)
    ang = positions.astype(F32)[:, None] * inv_freq
    ang = jnp.concatenate([ang, ang], axis=-1)
    t = positions.shape[0]
    cos = jnp.concatenate([jnp.ones((t, KR_LANE), F32), jnp.cos(ang), jnp.ones((t, LANE - KR_LANE - ROPE), F32)], axis=1)
    sin = jnp.concatenate([jnp.zeros((t, KR_LANE), F32), jnp.sin(ang), jnp.zeros((t, LANE - KR_LANE - ROPE), F32)], axis=1)
    return cos, sin


def _coords():
    return lax.axis_index("x"), lax.axis_index("y"), lax.axis_index("c")


HBM_SPEC = pl.BlockSpec(memory_space=pl.ANY)
VMEM_SPEC = pl.BlockSpec(memory_space=pltpu.VMEM)


def _gather8(v, *, name, comm=None):
    r, c = v.shape

    def body(v_ref, out_ref, send_sems, recv_sems, local_sem):
        x, y, cc = _coords()
        me = 4 * x + 2 * y + cc
        mine = pltpu.make_async_copy(v_ref, out_ref.at[me], local_sem)
        mine.start()
        copies = []
        for kk in range(1, N_DEV):
            peer = (x ^ (kk >> 2), y ^ ((kk >> 1) & 1), cc ^ (kk & 1))
            cp = pltpu.make_async_remote_copy(src_ref=v_ref, dst_ref=out_ref.at[me], send_sem=send_sems.at[kk - 1],
                                              recv_sem=recv_sems.at[kk - 1], device_id=peer, device_id_type=MESH)
            cp.start()
            copies.append(cp)
        for kk in range(1, N_DEV):
            peer_slot = me ^ kk
            pltpu.make_async_remote_copy(src_ref=v_ref, dst_ref=out_ref.at[peer_slot], send_sem=send_sems.at[kk - 1],
                                         recv_sem=recv_sems.at[kk - 1], device_id=(x, y, cc), device_id_type=MESH).wait_recv()
        for cp in copies:
            cp.wait_send()
        mine.wait()

    return _pcall(body, name=name, out_shape=_sds((N_DEV, r, c), v.dtype), in_specs=[VMEM_SPEC], out_specs=VMEM_SPEC,
                  scratch=[pltpu.SemaphoreType.DMA((N_DEV - 1,)), pltpu.SemaphoreType.DMA((N_DEV - 1,)), pltpu.SemaphoreType.DMA],
                  comm=comm)(v)


CHIP_RELS = ((1, 0), (0, 1), (1, 1))


def _comm_only(comm, *, name):
    _, outs = _pcall(lambda: None, name=name, out_shape=[], in_specs=[], out_specs=[], comm=comm)()
    return outs


def _gather_comm(shards):
    n = len(shards)

    def ici(ins, outs, sems, a, j, x, y, cc):
        half = ins[a].shape[0] // 2
        mine = pl.ds(cc * half, half)
        dx, dy = CHIP_RELS[j]
        return pltpu.make_async_remote_copy(src_ref=ins[a].at[mine], dst_ref=outs[a].at[2 * x + y, mine],
                                            send_sem=sems[0].at[a, j], recv_sem=sems[1].at[a, j],
                                            device_id=(x ^ dx, y ^ dy, cc), device_id_type=MESH)

    def d2d(ins, outs, sems, a, j, x, y, cc, half_of):
        half = ins[a].shape[0] // 2
        dx, dy = CHIP_RELS[j]
        landed = outs[a].at[2 * (x ^ dx) + (y ^ dy), pl.ds(half_of * half, half)]
        return pltpu.make_async_remote_copy(src_ref=landed, dst_ref=landed, send_sem=sems[2].at[a, j], recv_sem=sems[3].at[a, j],
                                            device_id=(x, y, 1 - cc), device_id_type=MESH)

    def own(ins, outs, sems, a, x, y):
        return pltpu.make_async_copy(ins[a], outs[a].at[2 * x + y], sems[4].at[a])

    def start(ins, outs, sems):
        x, y, cc = _coords()
        for a in range(n):
            own(ins, outs, sems, a, x, y).start()
            for j in range(3):
                ici(ins, outs, sems, a, j, x, y, cc).start()

    def finish(ins, outs, sems):
        x, y, cc = _coords()
        for a in range(n):
            for j in range(3):
                ici(ins, outs, sems, a, j, x, y, cc).wait_recv()
                d2d(ins, outs, sems, a, j, x, y, cc, cc).start()
        for a in range(n):
            for j in range(3):
                d2d(ins, outs, sems, a, j, x, y, cc, 1 - cc).wait_recv()
        for a in range(n):
            for j in range(3):
                ici(ins, outs, sems, a, j, x, y, cc).wait_send()
                d2d(ins, outs, sems, a, j, x, y, cc, cc).wait_send()
            own(ins, outs, sems, a, x, y).wait()

    dma = pltpu.SemaphoreType.DMA
    return _Comm(shards, [_sds((N_CHIP,) + s.shape, s.dtype) for s in shards],
                 [dma((n, 3)), dma((n, 3)), dma((n, 3)), dma((n, 3)), dma((n,))], start, finish)


def _swap_comm(parts):
    n = len(parts)

    def copy(ins, outs, sems, a):
        x, y, cc = _coords()
        half = ins[a].shape[1] // 2
        return pltpu.make_async_remote_copy(src_ref=ins[a].at[:, pl.ds((1 - cc) * half, half)], dst_ref=outs[a],
                                            send_sem=sems[0].at[a], recv_sem=sems[1].at[a], device_id=(x, y, 1 - cc),
                                            device_id_type=MESH)

    def start(ins, outs, sems):
        for a in range(n):
            copy(ins, outs, sems, a).start()

    def finish(ins, outs, sems):
        for a in range(n):
            copy(ins, outs, sems, a).wait()

    dma = pltpu.SemaphoreType.DMA
    return _Comm(parts, [_sds((p.shape[0], p.shape[1] // 2, p.shape[2]), p.dtype) for p in parts], [dma((n,)), dma((n,))],
                 start, finish)


def _add_half(full, other, cidx, *, name):
    nch, r, c = full.shape
    half = r // 2
    tr = _pick_rows(half)
    nbk = half // tr
    grid_spec = pltpu.PrefetchScalarGridSpec(
        num_scalar_prefetch=1, grid=(nch, nbk),
        in_specs=[pl.BlockSpec((1, tr, c), lambda j, i, cref: (j, cref[0] * nbk + i, 0)),
                  pl.BlockSpec((1, tr, c), lambda j, i, cref: (j, i, 0))],
        out_specs=pl.BlockSpec((1, tr, c), lambda j, i, cref: (j, i, 0)))

    def body(cref, a_ref, b_ref, o_ref):
        o_ref[...] = (a_ref[...] + b_ref[...]).astype(o_ref.dtype)

    return _pcall(body, name=name, out_shape=_sds((nch, half, c), MX), grid_spec=grid_spec)(cidx, full, other)


def _pick_rows(rows, target=512):
    best = None
    for t in range(16, min(rows, target) + 1, 16):
        if rows % t == 0:
            best = t
    return rows if best is None else best


def _exchange_comm(parts):
    n = len(parts)

    def send(ins, outs, sems, a, j, x, y, cc):
        dx, dy = CHIP_RELS[j]
        return pltpu.make_async_remote_copy(src_ref=ins[a].at[2 * (x ^ dx) + (y ^ dy)], dst_ref=outs[a].at[2 * x + y],
                                            send_sem=sems[0].at[a, j], recv_sem=sems[1].at[a, j],
                                            device_id=(x ^ dx, y ^ dy, cc), device_id_type=MESH)

    def landing(ins, outs, sems, a, j, x, y, cc):
        dx, dy = CHIP_RELS[j]
        peer_chip = 2 * (x ^ dx) + (y ^ dy)
        return pltpu.make_async_remote_copy(src_ref=ins[a].at[peer_chip], dst_ref=outs[a].at[peer_chip], send_sem=sems[0].at[a, j],
                                            recv_sem=sems[1].at[a, j], device_id=(x, y, cc), device_id_type=MESH)

    def own(ins, outs, sems, a, x, y):
        return pltpu.make_async_copy(ins[a].at[2 * x + y], outs[a].at[2 * x + y], sems[2].at[a])

    def start(ins, outs, sems):
        x, y, cc = _coords()
        for a in range(n):
            own(ins, outs, sems, a, x, y).start()
            for j in range(3):
                send(ins, outs, sems, a, j, x, y, cc).start()

    def finish(ins, outs, sems):
        x, y, cc = _coords()
        for a in range(n):
            for j in range(3):
                landing(ins, outs, sems, a, j, x, y, cc).wait_recv()
        for a in range(n):
            for j in range(3):
                send(ins, outs, sems, a, j, x, y, cc).wait_send()
            own(ins, outs, sems, a, x, y).wait()

    dma = pltpu.SemaphoreType.DMA
    return _Comm(parts, [_sds(p.shape, p.dtype) for p in parts], [dma((n, 3)), dma((n, 3)), dma((n,))], start, finish)


def _sum_chips(q, cidx, *, name):
    nch, h, c = q.shape
    tr = _pick_rows(h)
    nbk = h // tr
    grid_spec = pltpu.PrefetchScalarGridSpec(
        num_scalar_prefetch=1, grid=(nbk,),
        in_specs=[pl.BlockSpec((nch, tr, c), lambda i, cref: (0, i, 0))],
        out_specs=pl.BlockSpec((tr, c), lambda i, cref: (cref[0] * nbk + i, 0)))

    def body(cref, q_ref, o_ref):
        acc = q_ref[0].astype(F32) + q_ref[1].astype(F32)
        acc = acc + q_ref[2].astype(F32)
        o_ref[...] = acc + q_ref[3].astype(F32)

    return _pcall(body, name=name, out_shape=_sds((2 * h, c), F32), grid_spec=grid_spec)(cidx, q)


def _join_comm(fulls):
    n = len(fulls)

    def half_copy(outs, sems, a, which):
        x, y, cc = _coords()
        h = outs[a].shape[0] // 2
        rows = outs[a].at[pl.ds((cc if which == 0 else 1 - cc) * h, h)]
        return pltpu.make_async_remote_copy(src_ref=rows, dst_ref=rows, send_sem=sems[0].at[a], recv_sem=sems[1].at[a],
                                            device_id=(x, y, 1 - cc), device_id_type=MESH)

    def start(ins, outs, sems):
        for a in range(n):
            half_copy(outs, sems, a, 0).start()

    def finish(ins, outs, sems):
        for a in range(n):
            half_copy(outs, sems, a, 1).wait_recv()
        for a in range(n):
            half_copy(outs, sems, a, 0).wait_send()

    dma = pltpu.SemaphoreType.DMA
    return _Comm(fulls, [_sds(p.shape, p.dtype) for p in fulls], [dma((n,)), dma((n,))], start, finish,
                 aliases={a: a for a in range(n)})


def _ada_prologue(c, w, b, *, name, comm=None):
    nb, dm = c.shape
    n = w.shape[1]

    def body(c_ref, w_ref, b_ref, call_ref, land_ref, cpad, mod_s, g_send, g_recv, m_send, m_recv):
        x, y, cc = _coords()
        me = 4 * x + 2 * y + cc
        chip = 2 * x + y
        cpad[...] = jnp.zeros_like(cpad)
        cpad[0:nb, :] = c_ref[...]
        copies = []
        for kk in range(1, N_DEV):
            peer = (x ^ (kk >> 2), y ^ ((kk >> 1) & 1), cc ^ (kk & 1))
            cp = pltpu.make_async_remote_copy(src_ref=cpad, dst_ref=call_ref.at[me], send_sem=g_send.at[kk - 1],
                                              recv_sem=g_recv.at[kk - 1], device_id=peer, device_id_type=MESH)
            cp.start()
            copies.append(cp)
        call_ref[me] = cpad[...]
        for kk in range(1, N_DEV):
            pltpu.make_async_remote_copy(src_ref=cpad, dst_ref=call_ref.at[me ^ kk], send_sem=g_send.at[kk - 1],
                                         recv_sem=g_recv.at[kk - 1], device_id=(x, y, cc), device_id_type=MESH).wait_recv()
        cv = call_ref[...].reshape(N_DEV * SUBLANE, dm)
        act = (cv * _sigmoid(cv)).astype(MX)
        mod = jnp.dot(act, w_ref[...].astype(MX), preferred_element_type=F32) + b_ref[...]
        mod_s[...] = mod.reshape(N_DEV, SUBLANE, n)
        for j, (dx, dy) in enumerate(CHIP_RELS):
            cp = pltpu.make_async_remote_copy(src_ref=mod_s.at[4 * (x ^ dx) + 2 * (y ^ dy) + cc], dst_ref=land_ref.at[chip],
                                              send_sem=m_send.at[j], recv_sem=m_recv.at[j],
                                              device_id=(x ^ dx, y ^ dy, cc), device_id_type=MESH)
            cp.start()
            copies.append(cp)
        land_ref[chip] = mod_s[me]
        for j, (dx, dy) in enumerate(CHIP_RELS):
            pltpu.make_async_remote_copy(src_ref=mod_s.at[me], dst_ref=land_ref.at[2 * (x ^ dx) + (y ^ dy)], send_sem=m_send.at[j],
                                         recv_sem=m_recv.at[j], device_id=(x, y, cc), device_id_type=MESH).wait_recv()
        for cp in copies:
            cp.wait_send()

    dma = pltpu.SemaphoreType.DMA
    return _pcall(body, name=name, in_specs=[VMEM_SPEC] * 3, out_specs=[VMEM_SPEC] * 2,
                  out_shape=[_sds((N_DEV, SUBLANE, dm), F32), _sds((N_CHIP, SUBLANE, n), F32)],
                  scratch=[pltpu.VMEM((SUBLANE, dm), F32), pltpu.VMEM((N_DEV, SUBLANE, n), F32),
                           dma((N_DEV - 1,)), dma((N_DEV - 1,)), dma((3,)), dma((3,))], comm=comm)(c, w, b)


def _ada_bwd(c_all, dmod_cols, *, name):
    m, kdim = c_all.shape
    n = dmod_cols.shape[1]
    tn = _pick(n, 1152)

    def body(c_ref, d_ref, o_ref):
        cv = c_ref[...]
        act = (cv * _sigmoid(cv)).astype(MX)
        o_ref[...] = lax.dot_general(act, d_ref[...].astype(MX), (((0,), (0,)), ((), ())), preferred_element_type=F32)

    return _pcall(body, name=name, out_shape=_sds((kdim, n), F32), grid=(n // tn,),
                  in_specs=[pl.BlockSpec((m, kdim), lambda j: (0, 0)), pl.BlockSpec((m, tn), lambda j: (0, j))],
                  out_specs=pl.BlockSpec((kdim, tn), lambda j: (0, j)))(c_all, dmod_cols)


SLAB_W = 1024
RED_ROWS = 8


LOSS_LANE = SLAB_W - LANE


def _pack_stats(st1, st2, st3, st_fin, st_q, st_kv, st_prep, d_pool_scale, loss8, *, name):
    nb = st1.shape[0]
    dm_rows = -(-9 * nb // SUBLANE) * SUBLANE

    def body(s1, s2, s3, sf, sq, skv, sp, sps, loss_ref, o_ref):
        o_ref[...] = jnp.zeros_like(o_ref)
        for s in range(nb):
            rows = [s1[s, 0:1, :], s1[s, 1:2, :], s2[s, 3:4, :], s2[s, 0:1, :], s2[s, 1:2, :], s3[s, 3:4, :], s3[s, 0:1, :],
                    s3[s, 1:2, :], sf[s, 0:1, :]]
            for k, row in enumerate(rows):
                o_ref[9 * s + k:9 * s + k + 1, :] = row

        def over_seq(ref, r):
            acc = ref[0, r:r + 1, :]
            for s in range(1, nb):
                acc = acc + ref[s, r:r + 1, :]
            return acc

        o_ref[dm_rows + 0:dm_rows + 1, :] = over_seq(s1, 2)
        o_ref[dm_rows + 1:dm_rows + 2, :] = over_seq(s2, 2)
        o_ref[dm_rows + 2:dm_rows + 3, :] = over_seq(s3, 2)
        o_ref[dm_rows + 3:dm_rows + 4, 0:POOL_W] = over_seq(sps, 0)
        o_ref[dm_rows + 4:dm_rows + 5, 0:QL] = over_seq(sq, 0)
        o_ref[dm_rows + 5:dm_rows + 6, 0:KVL] = over_seq(skv, 0)
        o_ref[dm_rows + 6:dm_rows + 7, 0:LANE] = over_seq(sp, 0)
        o_ref[dm_rows + 7:dm_rows + 8, 0:LANE] = over_seq(sp, 1)
        o_ref[dm_rows + 7:dm_rows + 8, LOSS_LANE:] = loss_ref[0:1, :]

    return _pcall(body, name=name, out_shape=_sds((dm_rows + RED_ROWS, SLAB_W), F32), in_specs=[VMEM_SPEC] * 9,
                  out_specs=VMEM_SPEC)(st1, st2, st3, st_fin, st_q, st_kv, st_prep, d_pool_scale, loss8)


def _small_allreduce(slab, pool, *, name, comm=None):
    rows, w = slab.shape
    dm = rows - RED_ROWS
    prow, pw = pool.shape
    crow = RED_ROWS + 2 * dm

    def body(slab_ref, pool_ref, red_ref, dmod_ref, ptot_ref, sib_slab, sib_pool, chip_slab, chip_pool, land_slab, land_pool,
             a_send, a_recv, b_send, b_recv):
        x, y, cc = _coords()
        chip = 2 * x + y
        sib = (x, y, 1 - cc)
        to_sib = [pltpu.make_async_remote_copy(src_ref=slab_ref, dst_ref=sib_slab, send_sem=a_send.at[0], recv_sem=a_recv.at[0],
                                               device_id=sib, device_id_type=MESH),
                  pltpu.make_async_remote_copy(src_ref=pool_ref, dst_ref=sib_pool, send_sem=a_send.at[1], recv_sem=a_recv.at[1],
                                               device_id=sib, device_id_type=MESH)]
        for cp in to_sib:
            cp.start()
        for cp in to_sib:
            cp.wait()
        mine_dm, theirs_dm = slab_ref[0:dm, :], sib_slab[0:dm, :]
        chip_slab[0:RED_ROWS, :] = slab_ref[dm:, :] + sib_slab[dm:, :]
        chip_slab[RED_ROWS:RED_ROWS + dm, :] = jnp.where(cc == 0, mine_dm, theirs_dm)
        chip_slab[RED_ROWS + dm:, :] = jnp.where(cc == 0, theirs_dm, mine_dm)
        chip_pool[...] = pool_ref[...] + sib_pool[...]

        sends = []
        for j, (dx, dy) in enumerate(CHIP_RELS):
            peer = (x ^ dx, y ^ dy, cc)
            sends.append(pltpu.make_async_remote_copy(src_ref=chip_slab, dst_ref=land_slab.at[chip], send_sem=b_send.at[j, 0],
                                                      recv_sem=b_recv.at[j, 0], device_id=peer, device_id_type=MESH))
            sends.append(pltpu.make_async_remote_copy(src_ref=chip_pool, dst_ref=land_pool.at[chip], send_sem=b_send.at[j, 1],
                                                      recv_sem=b_recv.at[j, 1], device_id=peer, device_id_type=MESH))
        for cp in sends:
            cp.start()
        land_slab[chip] = chip_slab[...]
        land_pool[chip] = chip_pool[...]
        for j, (dx, dy) in enumerate(CHIP_RELS):
            peer_chip = 2 * (x ^ dx) + (y ^ dy)
            pltpu.make_async_remote_copy(src_ref=chip_slab, dst_ref=land_slab.at[peer_chip], send_sem=b_send.at[j, 0],
                                         recv_sem=b_recv.at[j, 0], device_id=(x, y, cc), device_id_type=MESH).wait_recv()
            pltpu.make_async_remote_copy(src_ref=chip_pool, dst_ref=land_pool.at[peer_chip], send_sem=b_send.at[j, 1],
                                         recv_sem=b_recv.at[j, 1], device_id=(x, y, cc), device_id_type=MESH).wait_recv()
        red = land_slab[0, 0:RED_ROWS, :]
        ptot = land_pool[0]
        for ch in range(1, N_CHIP):
            red = red + land_slab[ch, 0:RED_ROWS, :]
            ptot = ptot + land_pool[ch]
        red_ref[...] = red
        ptot_ref[...] = ptot
        for ch in range(N_CHIP):
            dmod_ref[2 * dm * ch:2 * dm * (ch + 1), :] = land_slab[ch, RED_ROWS:, :]
        for cp in sends:
            cp.wait_send()

    dma = pltpu.SemaphoreType.DMA
    return _pcall(body, name=name, in_specs=[VMEM_SPEC, VMEM_SPEC], out_specs=[VMEM_SPEC] * 3,
                  out_shape=[_sds((RED_ROWS, w), F32), _sds((N_DEV * dm, w), F32), _sds((prow, pw), F32)],
                  scratch=[pltpu.VMEM((rows, w), F32), pltpu.VMEM((prow, pw), F32), pltpu.VMEM((crow, w), F32),
                           pltpu.VMEM((prow, pw), F32), pltpu.VMEM((N_CHIP, crow, w), F32), pltpu.VMEM((N_CHIP, prow, pw), F32),
                           dma((2,)), dma((2,)), dma((3, 2)), dma((3, 2))], comm=comm)(slab, pool)


def _adamw_math(w, g, m, v):
    mn = ADAM_B1 * m + (1.0 - ADAM_B1) * g
    vn = ADAM_B2 * v + (1.0 - ADAM_B2) * (g * g)
    bc1 = 1.0 / (1.0 - ADAM_B1 ** ADAM_STEP)
    bc2 = 1.0 / (1.0 - ADAM_B2 ** ADAM_STEP)
    return -ADAM_LR * ((mn * bc1) / (jnp.sqrt(vn * bc2) + ADAM_EPS) + ADAM_WD * w), mn, vn


def _small_update(red, dmod_all, pool_total, nb, params, *, name):
    names = list(SMALL)
    dm = dmod_all.shape[0] // N_DEV

    def grad_of(nm, red_ref, dmod_ref, ptot_ref):
        if nm == "b_ada":
            acc = None
            for d in range(N_DEV):
                for s in range(nb):
                    blk = dmod_ref[d * dm + 9 * s:d * dm + 9 * s + 9, :]
                    acc = blk if acc is None else acc + blk
            return jnp.concatenate([acc[k:k + 1, :] for k in range(9)], axis=1)
        if nm == "pool_grp":
            return ptot_ref[...]
        row, lo, n = {"norm_ffn1": (0, 0, D), "norm_mix": (1, 0, D), "norm_ffn2": (2, 0, D), "pool_scale": (3, 0, POOL_W),
                      "q_a_norm": (4, 0, QL), "kv_a_norm": (5, 0, KVL), "q_norm_nope": (6, 0, NOPE),
                      "q_norm_rope": (6, NOPE, ROPE), "k_norm_nope": (7, 0, NOPE), "k_norm_rope": (7, KR_LANE, ROPE)}[nm]
        return red_ref[row:row + 1, lo:lo + n]

    def body(*refs):
        red_ref, dmod_ref, ptot_ref = refs[:3]
        ins = refs[3:3 + 3 * len(names)]
        outs = refs[3 + 3 * len(names):]
        outs[4 * len(names)][...] = red_ref[RED_ROWS - 1:RED_ROWS, LOSS_LANE:]
        for i, nm in enumerate(names):
            g = grad_of(nm, red_ref, dmod_ref, ptot_ref)
            d, mn, vn = _adamw_math(ins[3 * i][...], g, ins[3 * i + 1][...], ins[3 * i + 2][...])
            outs[4 * i][...] = g
            outs[4 * i + 1][...] = d
            outs[4 * i + 2][...] = mn
            outs[4 * i + 3][...] = vn

    flat_in = [a for nm in names for a in params[nm]]
    out_shape = [_sds(params[nm][0].shape, F32) for nm in names for _ in range(4)] + [_sds((1, LANE), F32)]
    res = _pcall(body, name=name, in_specs=[VMEM_SPEC] * (3 + len(flat_in)), out_specs=[VMEM_SPEC] * len(out_shape),
                 out_shape=out_shape)(red, dmod_all, pool_total, *flat_in)
    return {nm: tuple(res[4 * i:4 * i + 4]) for i, nm in enumerate(names)}, res[-1]


def _adamw(w, g, m, v, *, name, comm=None):
    r, c = w.shape
    tr = _pick_rows(r, 256)
    tc = c if tr < r else _pick(c, 256)
    spec = pl.BlockSpec((tr, tc), lambda i, j: (i, j))
    bc1 = 1.0 / (1.0 - ADAM_B1 ** ADAM_STEP)
    bc2 = 1.0 / (1.0 - ADAM_B2 ** ADAM_STEP)

    def body(w_ref, g_ref, m_ref, v_ref, d_ref, mo_ref, vo_ref):
        gv = g_ref[...]
        mn = ADAM_B1 * m_ref[...] + (1.0 - ADAM_B1) * gv
        vn = ADAM_B2 * v_ref[...] + (1.0 - ADAM_B2) * (gv * gv)
        mo_ref[...] = mn
        vo_ref[...] = vn
        d_ref[...] = -ADAM_LR * ((mn * bc1) / (jnp.sqrt(vn * bc2) + ADAM_EPS) + ADAM_WD * w_ref[...])

    out = _sds((r, c), F32)
    return _pcall(body, name=name, out_shape=[out, out, out], grid=(r // tr, c // tc), in_specs=[spec] * 4, out_specs=[spec] * 3,
                  comm=comm)(w, g, m, v)


BIG = ("w_ffn1_in", "w_ffn1_out", "w_in", "w_pool_proj", "w_q_up", "w_kv_up", "w_mla_proj", "w_out", "w_ffn2_in", "w_ffn2_out")
ROW_SHARDED = ("w_ffn1_out", "w_out", "w_ffn2_out")
KERNEL_NAME = {"w_ffn1_in": "ffn1_in", "w_ffn1_out": "ffn1_out", "w_in": "w_in", "w_pool_proj": "pool_proj", "w_q_up": "q_up",
               "w_kv_up": "kv_up", "w_mla_proj": "mla_proj", "w_out": "w_out", "w_ffn2_in": "ffn2_in", "w_ffn2_out": "ffn2_out"}
WEIGHTS = ("w_ada", "b_ada", "norm_ffn1", "w_ffn1_in", "w_ffn1_out", "norm_mix", "w_in", "pool_grp", "pool_scale", "w_pool_proj",
           "q_a_norm", "w_q_up", "kv_a_norm", "w_kv_up", "q_norm_nope", "q_norm_rope", "k_norm_nope", "k_norm_rope", "w_mla_proj",
           "w_out", "norm_ffn2", "w_ffn2_in", "w_ffn2_out")
SMALL = ("b_ada", "norm_ffn1", "norm_mix", "pool_grp", "pool_scale", "q_a_norm", "kv_a_norm", "q_norm_nope", "q_norm_rope",
         "k_norm_nope", "k_norm_rope", "norm_ffn2")
SMALL_SEQ = tuple(n for n in SMALL if n != "pool_grp")
SLAB_W = 1024


def _assemble(name, stacked):
    if name in ROW_SHARDED:
        return stacked.reshape(stacked.shape[0] * stacked.shape[1], stacked.shape[2])
    return jnp.transpose(stacked, (1, 0, 2)).reshape(stacked.shape[1], stacked.shape[0] * stacked.shape[2])


def _split(name, full):
    if name in ROW_SHARDED:
        return full.reshape(N_CHIP, full.shape[0] // N_CHIP, full.shape[1])
    return jnp.transpose(full.reshape(full.shape[0], N_CHIP, full.shape[1] // N_CHIP), (1, 0, 2))


GU = ("w_ffn1_in", "w_ffn2_in")
NARROW = ("w_in", "w_q_up")


def _to_rows(v, width=SLAB_W):
    flat = v.reshape(-1)
    rows = -(-flat.shape[0] // width)
    return jnp.pad(flat, (0, rows * width - flat.shape[0])).reshape(rows, width)


def _pack_small(parts, names=SMALL):
    rows, spans, at = [], {}, 0
    for name in names:
        r = _to_rows(parts[name])
        spans[name] = (at, parts[name].size, parts[name].shape)
        rows.append(r)
        at += r.shape[0]
    pad = (-at) % SUBLANE
    if pad:
        rows.append(jnp.zeros((pad, SLAB_W), F32))
    return jnp.concatenate(rows, axis=0), spans


def _unpack_small(slab, spans):
    out = {}
    for name, (at, size, shape) in spans.items():
        nrow = -(-size // SLAB_W)
        out[name] = slab[at:at + nrow].reshape(-1)[:size].reshape(shape)
    return out


MIX_SMALL = ("w_out", "w_pool_proj", "w_mla_proj", "w_q_up", "w_kv_up")
RIDES = {
    "ada_prologue": (("gather", ("w_ffn1_in",)),),
    "ffn1_up": (("gather", ("w_ffn1_out", "w_in")),),
    "mix_in": (("gather", MIX_SMALL),),
    "attn_fwd": (("gather", ("w_ffn2_in", "w_ffn2_out")),),
    "d_ffn2_h": (("swap", ("w_ffn2_out", "w_ffn2_in")),),
    "attn_bwd": (("exchange", ("w_ffn2_out", "w_ffn2_in")),),
    "d_mix_in": (("swap", MIX_SMALL),),
    "d_mix_h": (("exchange", MIX_SMALL), ("swap", ("w_in",))),
    "d_ffn1_out": (("exchange", ("w_in",)),),
    "ffn1_dact": (("swap", ("w_ffn1_out",)),),
    "d_ffn1_in_a": (("exchange", ("w_ffn1_out",)),),
    "d_ffn1_in_b": (("swap", ("w_ffn1_in@0",)),),
    "d_ffn1_h": (("exchange", ("w_ffn1_in@0",)), ("swap", ("w_ffn1_in@1",))),
    "bwd_norm1": (("exchange", ("w_ffn1_in@1",)),),
    "small_allreduce": (("join", tuple(n for n in BIG if n != "w_ffn1_in") + ("w_ffn1_in@0", "w_ffn1_in@1")),),
}


def _kname(n):
    base, _, part = n.partition("@")
    return KERNEL_NAME[base] + ("@" + part if part else "")


def _both(comms):
    if len(comms) == 1:
        return comms[0]
    ins, outs, sems, aliases, spans = [], [], [], {}, []
    for c in comms:
        spans.append((len(ins), len(c.ins), len(outs), len(c.out_shapes), len(sems), len(c.sems)))
        aliases.update({len(ins) + i: len(outs) + o for i, o in c.aliases.items()})
        ins, outs, sems = ins + c.ins, outs + c.out_shapes, sems + c.sems

    def each(which):
        def run(i_, o_, s_):
            for c, (ia, ni, oa, no, sa, ns) in zip(comms, spans):
                getattr(c, which)(i_[ia:ia + ni], o_[oa:oa + no], s_[sa:sa + ns])
        return run

    return _Comm(ins, outs, sems, each("start"), each("finish"), aliases)


class _ExchangePlan:
    def __init__(self, shards, cidx):
        self.shards, self.cidx = shards, cidx
        self.W, self.G, self.parts, self.pre, self.reduced, self.joined = {}, {}, {}, {}, {}, {}

    def grad(self, key, g):
        self.G[key] = g

    def rider(self, name):
        comms = []
        for kind, names in RIDES.get(name, ()):
            if kind == "gather":
                comms.append(_gather_comm([self.shards[n] for n in names]))
            elif kind == "swap":
                for n in names:
                    self.parts[n] = self._stacked(n)
                comms.append(_swap_comm([self.parts[n] for n in names]))
            elif kind == "exchange":
                comms.append(_exchange_comm([self.pre[n] for n in names]))
            else:
                comms.append(_join_comm([self.reduced[n] for n in names]))
        return _both(comms) if comms else None

    def landed(self, name, outs):
        at = 0
        for kind, names in RIDES[name]:
            for n, o in zip(names, outs[at:at + len(names)]):
                if kind == "gather":
                    self.W[KERNEL_NAME[n]] = self._to_kernel(n, o)
                elif kind == "swap":
                    self.pre[n] = _add_half(self.parts[n], o, self.cidx, name="rs_add_" + _kname(n))
                elif kind == "exchange":
                    self.reduced[n] = _sum_chips(o, self.cidx, name="rs_sum_" + _kname(n))
                else:
                    self.joined[n] = o
            at += len(names)

    @staticmethod
    def _to_kernel(n, stacked):
        if n in GU:
            return stacked
        full = _assemble(n, stacked)
        return {"w_in": _w_in_to_kernel, "w_q_up": _q_up_to_kernel, "w_kv_up": _kv_up_to_kernel}.get(n, lambda w: w)(full)

    def _stacked(self, n):
        g = self.G[_kname(n)]
        if n.partition("@")[0] in GU:
            return g
        full = {"w_in": _w_in_from_kernel, "w_q_up": _q_up_from_kernel, "w_kv_up": _kv_up_from_kernel}.get(n, lambda w: w)(g)
        return _split(n, full)


def kernel(x, c, positions, w_ada, b_ada, norm_ffn1, w_ffn1_in, w_ffn1_out, norm_mix, w_in, pool_grp, pool_scale, w_pool_proj, q_a_norm, w_q_up, kv_a_norm, w_kv_up, q_norm_nope, q_norm_rope, k_norm_nope, k_norm_rope, w_mla_proj, w_out, norm_ffn2, w_ffn2_in, w_ffn2_out, loss_target, m_w_ada, m_b_ada, m_norm_ffn1, m_w_ffn1_in, m_w_ffn1_out, m_norm_mix, m_w_in, m_pool_grp, m_pool_scale, m_w_pool_proj, m_q_a_norm, m_w_q_up, m_kv_a_norm, m_w_kv_up, m_q_norm_nope, m_q_norm_rope, m_k_norm_nope, m_k_norm_rope, m_w_mla_proj, m_w_out, m_norm_ffn2, m_w_ffn2_in, m_w_ffn2_out, v_w_ada, v_b_ada, v_norm_ffn1, v_w_ffn1_in, v_w_ffn1_out, v_norm_mix, v_w_in, v_pool_grp, v_pool_scale, v_w_pool_proj, v_q_a_norm, v_w_q_up, v_kv_a_norm, v_w_kv_up, v_q_norm_nope, v_q_norm_rope, v_k_norm_nope, v_k_norm_rope, v_w_mla_proj, v_w_out, v_norm_ffn2, v_w_ffn2_in, v_w_ffn2_out):
    args = dict(locals())
    wts = {n: args[n][0] for n in WEIGHTS}
    mom = {n: args["m_" + n][0] for n in WEIGHTS}
    var = {n: args["v_" + n][0] for n in WEIGHTS}
    nb, seq, dm = x.shape
    tokens = nb * seq
    xi, yi, ci = _coords()
    chip = 2 * xi + yi
    dev = 4 * xi + 2 * yi + ci

    cidx = ci.astype(jnp.int32).reshape(1)
    plan = _ExchangePlan({n: wts[n].astype(MX) for n in BIG}, cidx)
    plan.W["pool_grp"] = wts["pool_grp"].astype(MX)

    ncol = w_ada.shape[2]
    b_cols = lax.dynamic_slice_in_dim(wts["b_ada"].reshape(1, -1), chip * ncol, ncol, axis=1)
    c_slots, mod_slots = _run(plan, _ada_prologue, c, wts["w_ada"], b_cols, name="ada_prologue")
    c_all = c_slots[:, :nb].reshape(N_DEV * nb, dm)
    mod3 = jnp.transpose(mod_slots[:, :nb], (1, 0, 2)).reshape(nb, 9, dm)
    P = {"norm_ffn1": wts["norm_ffn1"].reshape(1, dm), "norm_mix": wts["norm_mix"].reshape(1, dm),
         "norm_ffn2": wts["norm_ffn2"].reshape(1, dm), "pool_scale": wts["pool_scale"].reshape(1, POOL_W),
         "q_a_norm": wts["q_a_norm"].reshape(1, QL), "kv_a_norm": wts["kv_a_norm"].reshape(1, KVL),
         "q_gain": _gain_slab(wts["q_norm_nope"].reshape(1, NOPE), wts["q_norm_rope"].reshape(1, ROPE)),
         "k_gain": _gain_slab(wts["k_norm_nope"].reshape(1, NOPE), wts["k_norm_rope"].reshape(1, ROPE))}
    cos, sin = _rope_tables(positions.reshape(tokens))

    loss8, grad_x, stats, d_pool_grp = _layer_fwd_bwd(x.reshape(tokens, dm), loss_target.reshape(tokens, dm), mod3, cos, sin, plan, P)

    slab = _pack_stats(*stats, loss8, name="pack_stats")
    red, dmod_rows, pool_total = _run(plan, _small_allreduce, slab, d_pool_grp.reshape(POOL_G * LANE, LANE), name="small_allreduce")
    grads = {n: plan.joined[n] for n in BIG if n != "w_ffn1_in"}
    grads["w_ffn1_in"] = jnp.concatenate([plan.joined["w_ffn1_in@0"], plan.joined["w_ffn1_in@1"]], axis=0)
    dm_rows = dmod_rows.shape[0] // N_DEV
    dmod_all = dmod_rows.reshape(N_DEV, dm_rows, SLAB_W)[:, :9 * nb].reshape(N_DEV * nb, 9 * dm)
    dmod_cols = lax.dynamic_slice_in_dim(dmod_all, chip * ncol, ncol, axis=1)
    grads["w_ada"] = _ada_bwd(c_all, dmod_cols, name="ada_bwd")

    delta, new_m, new_v = {}, {}, {}
    as2d = lambda a: a.reshape(POOL_G * LANE, LANE) if a.ndim == 4 else a.reshape(1, -1)
    upd, loss_row = _small_update(red, dmod_rows, pool_total, nb,
                                  {n: tuple(as2d(args[p + n]) for p in ("", "m_", "v_")) for n in SMALL}, name="small_update")
    loss = loss_row[0, 0]
    for n in SMALL:
        grads[n], delta[n], new_m[n], new_v[n] = upd[n]
    for n in ("w_ada",) + BIG:
        if n in NARROW:
            res = _adamw(wts[n].T, grads[n].T, mom[n].T, var[n].T, name="adamw_" + n)
            delta[n], new_m[n], new_v[n] = (r.T for r in res)
        else:
            delta[n], new_m[n], new_v[n] = _adamw(wts[n], grads[n], mom[n], var[n], name="adamw_" + n)

    def lead(a, n):
        return a.reshape((1,) + wts[n].shape)

    return (loss, grad_x.reshape(nb, seq, dm), *[lead(grads[n], n) for n in WEIGHTS], *[lead(delta[n], n) for n in WEIGHTS],
            *[lead(new_m[n], n) for n in WEIGHTS], *[lead(new_v[n], n) for n in WEIGHTS])
```

```python
import functools
import math

import jax
import jax.numpy as jnp
from jax import lax
from jax.experimental import pallas as pl
from jax.experimental.pallas import tpu as pltpu

F32 = jnp.float32
MX = jnp.bfloat16

D = 1024
DFF = 2816
NH = 8
POOL_W = 512
POOL_G = 4
QL = 384
KVL = 256
ROPE = 32
NOPE = 64
LANE = 128
SUBLANE = 8
EPS = 1e-6
ATTN_SCALE = 1.0 / math.sqrt(96.0)
NEG = -1e30

Z_UP, Z_QL, Z_KV, Z_KR, Z_GP, Z_GM, Z_W = 0, 512, 896, 1152, 1280, 2304, 3328
KR_LANE = 64
LAT_W = 1280

ADAM_LR, ADAM_B1, ADAM_B2, ADAM_EPS, ADAM_WD, ADAM_STEP = 0.001, 0.9, 0.999, 1e-08, 0.01, 10

VMEM_LIMIT = 48 * 1024 * 1024
MESH = pl.DeviceIdType.MESH
N_DEV = 8
N_CHIP = 4


class _Comm:
    def __init__(self, ins, out_shapes, sems, start, finish, aliases=None):
        self.ins, self.out_shapes, self.sems = list(ins), list(out_shapes), list(sems)
        self.start, self.finish = start, finish
        self.aliases = aliases or {}


def _pcall(body, *, name, out_shape, grid=(), in_specs=None, out_specs=None, scratch=(), grid_spec=None, aliases=None,
           comm=None):
    params = pltpu.CompilerParams(vmem_limit_bytes=VMEM_LIMIT)
    kw = dict(name=name, compiler_params=params)
    if comm is None:
        if aliases:
            kw["input_output_aliases"] = aliases
        if grid_spec is not None:
            return pl.pallas_call(body, grid_spec=grid_spec, out_shape=out_shape, **kw)
        return pl.pallas_call(body, grid=grid, in_specs=in_specs, out_specs=out_specs, scratch_shapes=scratch,
                              out_shape=out_shape, **kw)
    single = not isinstance(out_shape, (list, tuple))
    outs = [out_shape] if single else list(out_shape)
    ospecs = [out_specs] if single else list(out_specs)
    n_in, n_out, n_ci, n_co, n_scr = len(in_specs), len(outs), len(comm.ins), len(comm.out_shapes), len(scratch)
    io = dict(aliases or {})
    io.update({n_in + i: n_out + o for i, o in comm.aliases.items()})

    def riding(*refs):
        ins, cins = refs[:n_in], refs[n_in:n_in + n_ci]
        at = n_in + n_ci
        os_, couts = refs[at:at + n_out], refs[at + n_out:at + n_out + n_co]
        at += n_out + n_co
        scr, csems = refs[at:at + n_scr], refs[at + n_scr:]
        if grid:
            first = functools.reduce(jnp.logical_and, [pl.program_id(d) == 0 for d in range(len(grid))])
            last = functools.reduce(jnp.logical_and, [pl.program_id(d) == grid[d] - 1 for d in range(len(grid))])
            pl.when(first)(lambda: comm.start(cins, couts, csems))
            body(*ins, *os_, *scr)
            pl.when(last)(lambda: comm.finish(cins, couts, csems))
        else:
            comm.start(cins, couts, csems)
            body(*ins, *os_, *scr)
            comm.finish(cins, couts, csems)

    call = pl.pallas_call(riding, grid=grid, in_specs=list(in_specs) + [HBM_SPEC] * n_ci, out_specs=ospecs + [HBM_SPEC] * n_co,
                          out_shape=outs + comm.out_shapes, scratch_shapes=list(scratch) + comm.sems,
                          input_output_aliases=io, **kw)

    def run(*args):
        res = call(*args, *comm.ins)
        main = list(res[:n_out])
        return (main[0] if single else main), list(res[n_out:])

    return run


def _pick(dim, target):
    best = None
    for t in range(LANE, min(dim, target) + 1, LANE):
        if dim % t == 0:
            best = t
    return dim if best is None else best


def _sds(shape, dtype):
    return jax.ShapeDtypeStruct(shape, dtype)


def _dw(a, g, *, name, tn=1024, comm=None):
    return _mm(a, g, mode="tn", out_dtype=F32, name=name, tm=256, tn=tn, tk=a.shape[0], n_outer=True, comm=comm)


def _mm(a, b, *, mode, out_dtype, name, tm=1024, tn=1024, tk=4096, n_outer=False, comm=None):
    if mode == "nn":
        (M, K), (K2, N) = a.shape, b.shape
    elif mode == "nt":
        (M, K), (N, K2) = a.shape, b.shape
    else:
        (K, M), (K2, N) = a.shape, b.shape
    assert K == K2, (name, a.shape, b.shape)
    tm, tn, tk = _pick(M, tm), _pick(N, tn), _pick(K, tk)
    nk = K // tk
    if n_outer:
        ij = lambda g0, g1: (g1, g0)
        grid = (N // tn, M // tm, nk)
    else:
        ij = lambda g0, g1: (g0, g1)
        grid = (M // tm, N // tn, nk)
    if mode == "tn":
        a_spec = pl.BlockSpec((tk, tm), lambda g0, g1, k: (k, ij(g0, g1)[0]))
    else:
        a_spec = pl.BlockSpec((tm, tk), lambda g0, g1, k: (ij(g0, g1)[0], k))
    if mode == "nt":
        b_spec = pl.BlockSpec((tn, tk), lambda g0, g1, k: (ij(g0, g1)[1], k))
    else:
        b_spec = pl.BlockSpec((tk, tn), lambda g0, g1, k: (k, ij(g0, g1)[1]))
    o_spec = pl.BlockSpec((tm, tn), lambda g0, g1, k: ij(g0, g1))
    dn = {"nn": (((1,), (0,)), ((), ())), "nt": (((1,), (1,)), ((), ())), "tn": (((0,), (0,)), ((), ()))}[mode]

    def dot(a_ref, b_ref):
        return lax.dot_general(a_ref[...].astype(MX), b_ref[...].astype(MX), dn, preferred_element_type=F32)

    def body_one(a_ref, b_ref, o_ref):
        o_ref[...] = dot(a_ref, b_ref).astype(o_ref.dtype)

    def body_acc(a_ref, b_ref, o_ref, acc_ref):
        k = pl.program_id(2)
        part = dot(a_ref, b_ref)

        @pl.when(k == 0)
        def _():
            acc_ref[...] = part

        @pl.when(k > 0)
        def _():
            acc_ref[...] += part

        @pl.when(k == nk - 1)
        def _():
            o_ref[...] = acc_ref[...].astype(o_ref.dtype)

    return _pcall(body_one if nk == 1 else body_acc, name=name, out_shape=_sds((M, N), out_dtype), grid=grid,
                  in_specs=[a_spec, b_spec], out_specs=o_spec, scratch=[] if nk == 1 else [pltpu.VMEM((tm, tn), F32)],
                  comm=comm)(a, b)


def _gu_shard(q):
    return (q % 2) * 2 + q // 2


def _ffn_up(h, w_st, *, name, tm=512, comm=None):
    T, dm = h.shape
    hw = w_st.shape[2]
    tm = _pick(T, tm)

    def body(h_ref, wg_ref, wu_ref, gu_ref, a_ref):
        hv = h_ref[...]
        g = jnp.dot(hv, wg_ref[0], preferred_element_type=F32)
        u = jnp.dot(hv, wu_ref[0], preferred_element_type=F32)
        gu_ref[:, :hw] = g.astype(gu_ref.dtype)
        gu_ref[:, hw:] = u.astype(gu_ref.dtype)
        a_ref[...] = (g * _sigmoid(g) * u).astype(a_ref.dtype)

    return _pcall(body, name=name, grid=(T // tm, 2),
                  in_specs=[pl.BlockSpec((tm, dm), lambda i, j: (i, 0)), pl.BlockSpec((1, dm, hw), lambda i, j: (j, 0, 0)),
                            pl.BlockSpec((1, dm, hw), lambda i, j: (2 + j, 0, 0))],
                  out_specs=[pl.BlockSpec((tm, 2 * hw), lambda i, j: (i, j)), pl.BlockSpec((tm, hw), lambda i, j: (i, j))],
                  out_shape=[_sds((T, 4 * hw), MX), _sds((T, 2 * hw), MX)], comm=comm)(h, w_st, w_st)


def _ffn_up_first(x, mod3, gain, w_lo, *, sub, name, tm=512, comm=None):
    T, dm = x.shape
    hw = w_lo.shape[2]
    tm = _pick(T, tm)
    tps = (T // mod3.shape[0]) // tm

    def body(x_ref, mod_ref, n_ref, wg_ref, wu_ref, h_ref, gu_ref, a_ref):
        xv = x_ref[...]
        xn = xv * _rsq(xv) * n_ref[...]
        hv = (xn * (1.0 + mod_ref[0, 3 * sub + 1:3 * sub + 2, :]) + mod_ref[0, 3 * sub:3 * sub + 1, :]).astype(h_ref.dtype)
        h_ref[...] = hv
        g = jnp.dot(hv, wg_ref[0], preferred_element_type=F32)
        u = jnp.dot(hv, wu_ref[0], preferred_element_type=F32)
        gu_ref[:, :hw] = g.astype(gu_ref.dtype)
        gu_ref[:, hw:] = u.astype(gu_ref.dtype)
        a_ref[...] = (g * _sigmoid(g) * u).astype(a_ref.dtype)

    return _pcall(body, name=name, grid=(T // tm,),
                  in_specs=[_row_spec(tm, dm), pl.BlockSpec((1, 9, dm), lambda i: (i // tps, 0, 0)),
                            pl.BlockSpec((1, dm), lambda i: (0, 0)), pl.BlockSpec((1, dm, hw), lambda i: (0, 0, 0)),
                            pl.BlockSpec((1, dm, hw), lambda i: (2, 0, 0))],
                  out_specs=[_row_spec(tm, dm), pl.BlockSpec((tm, 2 * hw), lambda i: (i, 0)), pl.BlockSpec((tm, hw), lambda i: (i, 0))],
                  out_shape=[_sds((T, dm), MX), _sds((T, 4 * hw), MX), _sds((T, 2 * hw), MX)],
                  comm=comm)(x, mod3, gain, w_lo, w_lo)


def _ffn_up_second(h, w_hi, gu, a, *, name, tm=512, comm=None):
    T, dm = h.shape
    hw = w_hi.shape[2]
    tm = _pick(T, tm)

    def body(h_ref, wg_ref, wu_ref, gu_in, a_in, gu_ref, a_ref):
        hv = h_ref[...]
        g = jnp.dot(hv, wg_ref[0], preferred_element_type=F32)
        u = jnp.dot(hv, wu_ref[0], preferred_element_type=F32)
        gu_ref[:, :hw] = g.astype(gu_ref.dtype)
        gu_ref[:, hw:] = u.astype(gu_ref.dtype)
        a_ref[...] = (g * _sigmoid(g) * u).astype(a_ref.dtype)

    return _pcall(body, name=name, grid=(T // tm,),
                  in_specs=[_row_spec(tm, dm), pl.BlockSpec((1, dm, hw), lambda i: (1, 0, 0)),
                            pl.BlockSpec((1, dm, hw), lambda i: (3, 0, 0)), HBM_SPEC, HBM_SPEC],
                  out_specs=[pl.BlockSpec((tm, 2 * hw), lambda i: (i, 1)), pl.BlockSpec((tm, hw), lambda i: (i, 1))],
                  out_shape=[_sds(gu.shape, gu.dtype), _sds(a.shape, a.dtype)], aliases={3: 0, 4: 1},
                  comm=comm)(h, w_hi, w_hi, gu, a)


def _ffn_dact(df, gu, w_out, *, name, tm=512, comm=None):
    T, dm = df.shape
    hw = gu.shape[1] // 4
    tm = _pick(T, tm)

    def body(df_ref, gu_ref, wo_ref, dgu_ref):
        da = lax.dot_general(df_ref[...], wo_ref[...], (((1,), (1,)), ((), ())), preferred_element_type=F32)
        g = gu_ref[:, :hw].astype(F32)
        u = gu_ref[:, hw:].astype(F32)
        s = _sigmoid(g)
        dgu_ref[:, :hw] = (da * u * (s * (1.0 + g * (1.0 - s)))).astype(dgu_ref.dtype)
        dgu_ref[:, hw:] = (da * (g * s)).astype(dgu_ref.dtype)

    return _pcall(body, name=name, grid=(T // tm, 2),
                  in_specs=[pl.BlockSpec((tm, dm), lambda i, j: (i, 0)), pl.BlockSpec((tm, 2 * hw), lambda i, j: (i, j)),
                            pl.BlockSpec((hw, dm), lambda i, j: (j, 0))],
                  out_specs=pl.BlockSpec((tm, 2 * hw), lambda i, j: (i, j)), out_shape=_sds(gu.shape, MX),
                  comm=comm)(df, gu, w_out)


def _ffn_dh(dgu, w_lo, w_hi, *, name, tm=1024, comm=None):
    T = dgu.shape[0]
    _, dm, hw = w_lo.shape
    tm = _pick(T, tm)

    def body(d_ref, lo_ref, hi_ref, o_ref, acc_ref):
        q = pl.program_id(1)

        def part(w_ref):
            return lax.dot_general(d_ref[...], w_ref[0], (((1,), (1,)), ((), ())), preferred_element_type=F32)

        @pl.when(q == 0)
        def _():
            acc_ref[...] = part(lo_ref)

        @pl.when(q == 1)
        def _():
            acc_ref[...] += part(lo_ref)

        @pl.when(q == 2)
        def _():
            acc_ref[...] += part(hi_ref)

        @pl.when(q == 3)
        def _():
            o_ref[...] = acc_ref[...] + part(hi_ref)

    return _pcall(body, name=name, grid=(T // tm, 4),
                  in_specs=[pl.BlockSpec((tm, hw), lambda i, q: (i, q)),
                            pl.BlockSpec((1, dm, hw), lambda i, q: (jnp.where(q < 2, _gu_shard(q), 2), 0, 0)),
                            pl.BlockSpec((1, dm, hw), lambda i, q: (jnp.where(q < 2, 1, _gu_shard(q)), 0, 0))],
                  out_specs=pl.BlockSpec((tm, dm), lambda i, q: (i, 0)), out_shape=_sds((T, dm), F32),
                  scratch=[pltpu.VMEM((tm, dm), F32)], comm=comm)(dgu, w_lo, w_hi)


def _ffn_dw_in(h, dgu, *, name, rows=None, tm=256, comm=None):
    T, dm = h.shape
    hw = dgu.shape[1] // 4
    first, count = rows if rows is not None else (0, dm)
    tm = _pick(count, tm)
    skip = first // tm

    def body(h_ref, d_ref, o_ref):
        o_ref[0] = lax.dot_general(h_ref[...], d_ref[...], (((0,), (0,)), ((), ())), preferred_element_type=F32)

    return _pcall(body, name=name, grid=(4, count // tm),
                  in_specs=[pl.BlockSpec((T, tm), lambda q, i: (0, skip + i)), pl.BlockSpec((T, hw), lambda q, i: (0, q))],
                  out_specs=pl.BlockSpec((1, tm, hw), lambda q, i: (_gu_shard(q), i, 0)),
                  out_shape=_sds((4, count, hw), F32), comm=comm)(h, dgu)


def _rsq(x):
    return lax.rsqrt(jnp.mean(x * x, axis=-1, keepdims=True) + EPS)


def _sigmoid(x):
    return 1.0 / (1.0 + jnp.exp(-x))


def _row_spec(tm, w):
    return pl.BlockSpec((tm, w), lambda i: (i, 0))


def _fwd_block(x_prev, f_prev, mod3, gain, *, sub, coef, name, tm=256):
    T, dm = x_prev.shape
    tps = (T // mod3.shape[0]) // tm
    has_f = f_prev is not None
    mod_spec = pl.BlockSpec((1, 9, dm), lambda i: (i // tps, 0, 0))
    vec_spec = pl.BlockSpec((1, dm), lambda i: (0, 0))

    def body(*refs):
        if has_f:
            x_ref, f_ref, mod_ref, n_ref, xo_ref, h_ref = refs
            x = x_ref[...] + coef * mod_ref[0, 3 * sub - 1:3 * sub, :] * f_ref[...]
            xo_ref[...] = x
        else:
            x_ref, mod_ref, n_ref, h_ref = refs
            x = x_ref[...]
        xn = x * _rsq(x) * n_ref[...]
        h = xn * (1.0 + mod_ref[0, 3 * sub + 1:3 * sub + 2, :]) + mod_ref[0, 3 * sub:3 * sub + 1, :]
        h_ref[...] = h.astype(h_ref.dtype)

    row = _row_spec(tm, dm)
    if has_f:
        return _pcall(body, name=name, grid=(T // tm,), in_specs=[row, row, mod_spec, vec_spec], out_specs=[row, row],
                      out_shape=[_sds((T, dm), F32), _sds((T, dm), MX)])(x_prev, f_prev, mod3, gain)
    return _pcall(body, name=name, grid=(T // tm,), in_specs=[row, mod_spec, vec_spec], out_specs=row,
                  out_shape=_sds((T, dm), MX))(x_prev, mod3, gain)


def _mm_resid_norm(a, w, x_prev, mod3, gain, *, sub, coef, name, tm=512, comm=None):
    T, k = a.shape
    dm = w.shape[1]
    tm = _pick(T, tm)
    tps = (T // mod3.shape[0]) // tm

    def body(a_ref, w_ref, x_ref, mod_ref, n_ref, f_ref, xo_ref, h_ref):
        f = jnp.dot(a_ref[...], w_ref[...], preferred_element_type=F32)
        f_ref[...] = f
        x = x_ref[...] + coef * mod_ref[0, 3 * sub - 1:3 * sub, :] * f
        xo_ref[...] = x
        xn = x * _rsq(x) * n_ref[...]
        h_ref[...] = (xn * (1.0 + mod_ref[0, 3 * sub + 1:3 * sub + 2, :]) + mod_ref[0, 3 * sub:3 * sub + 1, :]).astype(h_ref.dtype)

    row = _row_spec(tm, dm)
    return _pcall(body, name=name, grid=(T // tm,),
                  in_specs=[_row_spec(tm, k), pl.BlockSpec((k, dm), lambda i: (0, 0)), row,
                            pl.BlockSpec((1, 9, dm), lambda i: (i // tps, 0, 0)), pl.BlockSpec((1, dm), lambda i: (0, 0))],
                  out_specs=[row, row, row], out_shape=[_sds((T, dm), F32), _sds((T, dm), F32), _sds((T, dm), MX)],
                  comm=comm)(a, w, x_prev, mod3, gain)


def _mm_loss(a, w, x2, tgt, mod3, *, name, tm=512):
    T, k = a.shape
    dm = w.shape[1]
    tm = _pick(T, tm)
    tps = (T // mod3.shape[0]) // tm
    mod_spec = pl.BlockSpec((1, 9, dm), lambda i: (i // tps, 0, 0))
    row = _row_spec(tm, dm)
    stat_spec = pl.BlockSpec((1, SUBLANE, dm), lambda i: (i // tps, 0, 0))
    loss_spec = pl.BlockSpec((SUBLANE, LANE), lambda i: (0, 0))

    def body(a_ref, w_ref, x_ref, t_ref, mod_ref, dy_ref, df_ref, st_ref, loss_ref):
        i = pl.program_id(0)
        g = mod_ref[0, 8:9, :]
        f = jnp.dot(a_ref[...], w_ref[...], preferred_element_type=F32)
        err = x_ref[...] + 0.5 * g * f - t_ref[...]
        dy = err * (1.0 / dm)
        dy_ref[...] = dy
        df_ref[...] = (0.5 * g * dy).astype(df_ref.dtype)
        dgate = jnp.sum(0.5 * dy * f, axis=0, keepdims=True)
        part = 0.5 * jnp.sum(jnp.sum(err * err, axis=0, keepdims=True), axis=1, keepdims=True) * (1.0 / dm)

        @pl.when(i % tps == 0)
        def _():
            st_ref[...] = jnp.zeros_like(st_ref)

        @pl.when(i == 0)
        def _():
            loss_ref[...] = jnp.zeros_like(loss_ref)

        st_ref[0, 0:1, :] += dgate
        loss_ref[...] += jnp.broadcast_to(part, loss_ref.shape)

    return _pcall(body, name=name, grid=(T // tm,),
                  in_specs=[_row_spec(tm, k), pl.BlockSpec((k, dm), lambda i: (0, 0)), row, row, mod_spec],
                  out_specs=[row, row, stat_spec, loss_spec],
                  out_shape=[_sds((T, dm), F32), _sds((T, dm), MX), _sds((mod3.shape[0], SUBLANE, dm), F32),
                             _sds((SUBLANE, LANE), F32)])(a, w, x2, tgt, mod3)


def _norm_bwd_tail(dhv, x_ref, dxi_ref, f_ref, mod_ref, n_ref, dx_ref, df_ref, st_ref, *, first, sub, coef):
    x = x_ref[...]
    r = _rsq(x)
    xhat = x * r
    n = n_ref[...]
    d_shift = jnp.sum(dhv, axis=0, keepdims=True)
    d_scale = jnp.sum(dhv * (xhat * n), axis=0, keepdims=True)
    dxn = dhv * (1.0 + mod_ref[0, 3 * sub + 1:3 * sub + 2, :])
    d_gain = jnp.sum(dxn * xhat, axis=0, keepdims=True)
    dxhat = dxn * n
    dx = dxi_ref[...] + r * (dxhat - xhat * jnp.mean(dxhat * xhat, axis=-1, keepdims=True))
    dx_ref[...] = dx

    @pl.when(first)
    def _():
        st_ref[...] = jnp.zeros_like(st_ref)

    st_ref[0, 0:1, :] += d_shift
    st_ref[0, 1:2, :] += d_scale
    st_ref[0, 2:3, :] += d_gain
    if f_ref is not None:
        st_ref[0, 3:4, :] += jnp.sum(coef * dx * f_ref[...], axis=0, keepdims=True)
        df_ref[...] = (coef * mod_ref[0, 3 * sub - 1:3 * sub, :] * dx).astype(df_ref.dtype)


def _ffn_dh_norm(dgu, w_st, x_cur, dx_in, f_prev, mod3, gain, *, sub, coef, name, tm=512, comm=None):
    T = dgu.shape[0]
    _, dm, hw = w_st.shape
    tm = _pick(T, tm)
    nb = mod3.shape[0]
    tps = (T // nb) // tm

    def body(d_ref, w_ref, x_ref, dxi_ref, f_ref, mod_ref, n_ref, dx_ref, df_ref, st_ref, acc_ref):
        i = pl.program_id(0)
        q = pl.program_id(1)
        part = lax.dot_general(d_ref[...], w_ref[0], (((1,), (1,)), ((), ())), preferred_element_type=F32)

        @pl.when(q == 0)
        def _():
            acc_ref[...] = part

        @pl.when(q > 0)
        def _():
            acc_ref[...] += part

        @pl.when(q == 3)
        def _():
            _norm_bwd_tail(acc_ref[...], x_ref, dxi_ref, f_ref, mod_ref, n_ref, dx_ref, df_ref, st_ref,
                           first=i % tps == 0, sub=sub, coef=coef)

    row = pl.BlockSpec((tm, dm), lambda i, q: (i, 0))
    return _pcall(body, name=name, grid=(T // tm, 4),
                  in_specs=[pl.BlockSpec((tm, hw), lambda i, q: (i, q)), pl.BlockSpec((1, dm, hw), lambda i, q: (_gu_shard(q), 0, 0)),
                            row, row, row, pl.BlockSpec((1, 9, dm), lambda i, q: (i // tps, 0, 0)),
                            pl.BlockSpec((1, dm), lambda i, q: (0, 0))],
                  out_specs=[row, row, pl.BlockSpec((1, SUBLANE, dm), lambda i, q: (i // tps, 0, 0))],
                  out_shape=[_sds((T, dm), F32), _sds((T, dm), MX), _sds((nb, SUBLANE, dm), F32)],
                  scratch=[pltpu.VMEM((tm, dm), F32)], comm=comm)(dgu, w_st, x_cur, dx_in, f_prev, mod3, gain)


def _mm_nt_norm(a, b, x_cur, dx_in, f_prev, mod3, gain, *, sub, coef, name, tm=512, comm=None):
    T, k = a.shape
    dm = b.shape[0]
    tm = _pick(T, tm)
    nb = mod3.shape[0]
    tps = (T // nb) // tm

    def body(a_ref, b_ref, x_ref, dxi_ref, f_ref, mod_ref, n_ref, dx_ref, df_ref, st_ref):
        dh = lax.dot_general(a_ref[...], b_ref[...], (((1,), (1,)), ((), ())), preferred_element_type=F32)
        _norm_bwd_tail(dh, x_ref, dxi_ref, f_ref, mod_ref, n_ref, dx_ref, df_ref, st_ref,
                       first=pl.program_id(0) % tps == 0, sub=sub, coef=coef)

    row = _row_spec(tm, dm)
    return _pcall(body, name=name, grid=(T // tm,),
                  in_specs=[_row_spec(tm, k), pl.BlockSpec((dm, k), lambda i: (0, 0)), row, row, row,
                            pl.BlockSpec((1, 9, dm), lambda i: (i // tps, 0, 0)), pl.BlockSpec((1, dm), lambda i: (0, 0))],
                  out_specs=[row, row, pl.BlockSpec((1, SUBLANE, dm), lambda i: (i // tps, 0, 0))],
                  out_shape=[_sds((T, dm), F32), _sds((T, dm), MX), _sds((nb, SUBLANE, dm), F32)],
                  comm=comm)(a, b, x_cur, dx_in, f_prev, mod3, gain)


def _bwd_block(x_cur, dh, dx_in, f_prev, mod3, gain, *, sub, coef, name, tm=256, comm=None):
    T, dm = x_cur.shape
    nb = mod3.shape[0]
    tps = (T // nb) // tm
    has_f = f_prev is not None
    mod_spec = pl.BlockSpec((1, 9, dm), lambda i: (i // tps, 0, 0))
    vec_spec = pl.BlockSpec((1, dm), lambda i: (0, 0))
    stat_spec = pl.BlockSpec((1, SUBLANE, dm), lambda i: (i // tps, 0, 0))
    row = _row_spec(tm, dm)

    def body(*refs):
        if has_f:
            x_ref, dh_ref, dxi_ref, f_ref, mod_ref, n_ref, dx_ref, df_ref, st_ref = refs
        else:
            x_ref, dh_ref, dxi_ref, mod_ref, n_ref, dx_ref, st_ref = refs
            f_ref = df_ref = None
        _norm_bwd_tail(dh_ref[...], x_ref, dxi_ref, f_ref, mod_ref, n_ref, dx_ref, df_ref, st_ref,
                       first=pl.program_id(0) % tps == 0, sub=sub, coef=coef)

    st_shape = _sds((nb, SUBLANE, dm), F32)
    if has_f:
        return _pcall(body, name=name, grid=(T // tm,), in_specs=[row, row, row, row, mod_spec, vec_spec],
                      out_specs=[row, row, stat_spec],
                      out_shape=[_sds((T, dm), F32), _sds((T, dm), MX), st_shape], comm=comm)(x_cur, dh, dx_in, f_prev, mod3, gain)
    return _pcall(body, name=name, grid=(T // tm,), in_specs=[row, row, row, mod_spec, vec_spec],
                  out_specs=[row, stat_spec], out_shape=[_sds((T, dm), F32), st_shape], comm=comm)(x_cur, dh, dx_in, mod3, gain)


def _merge_fwd(z, br_pool, br_mla, *, name, tm=256):
    T = z.shape[0]

    def body(z_ref, bp_ref, bm_ref, o_ref):
        gp = z_ref[:, Z_GP:Z_GP + D]
        gm = z_ref[:, Z_GM:Z_GM + D]
        o_ref[...] = (_sigmoid(gp) * bp_ref[...] + _sigmoid(gm) * bm_ref[...]).astype(o_ref.dtype)

    return _pcall(body, name=name, grid=(T // tm,), in_specs=[_row_spec(tm, Z_W), _row_spec(tm, D), _row_spec(tm, D)],
                  out_specs=_row_spec(tm, D), out_shape=_sds((T, D), MX))(z, br_pool, br_mla)


def _merge_bwd(z, br_pool, br_mla, dmerged, *, name, tm=256):
    T = z.shape[0]

    def body(z_ref, bp_ref, bm_ref, dm_ref, dbp_ref, dbm_ref, dg_ref):
        dm = dm_ref[...]
        sp = _sigmoid(z_ref[:, Z_GP:Z_GP + D])
        sm = _sigmoid(z_ref[:, Z_GM:Z_GM + D])
        dbp_ref[...] = (dm * sp).astype(dbp_ref.dtype)
        dbm_ref[...] = (dm * sm).astype(dbm_ref.dtype)
        dg_ref[:, :D] = (dm * bp_ref[...] * sp * (1.0 - sp)).astype(dg_ref.dtype)
        dg_ref[:, D:] = (dm * bm_ref[...] * sm * (1.0 - sm)).astype(dg_ref.dtype)

    return _pcall(body, name=name, grid=(T // tm,),
                  in_specs=[_row_spec(tm, Z_W), _row_spec(tm, D), _row_spec(tm, D), _row_spec(tm, D)],
                  out_specs=[_row_spec(tm, D), _row_spec(tm, D), _row_spec(tm, 2 * D)],
                  out_shape=[_sds((T, D), MX), _sds((T, D), MX), _sds((T, 2 * D), MX)])(z, br_pool, br_mla, dmerged)


def _shift_down(x, k, row):
    return jnp.where(row >= k, pltpu.roll(x, k, 0), 0.0)


def _shift_up(x, k, row, n):
    return jnp.where(row < n - k, pltpu.roll(x, n - k, 0), 0.0)


def _pool_fwd(z, pool_grp, pool_scale, *, nb, name):
    T = z.shape[0]
    S = T // nb
    blk = pl.BlockSpec((S, LANE), lambda b, g: (b, g))

    def body(u_ref, w_ref, s_ref, pooled_ref, mixed_ref, scaled_ref):
        g = pl.program_id(1)
        u = u_ref[...]
        row = lax.broadcasted_iota(jnp.int32, u.shape, 0)
        s2 = u + _shift_down(u, 1, row)
        s4 = s2 + _shift_down(s2, 2, row)
        s8 = s4 + _shift_down(s4, 4, row)
        s16 = s8 + _shift_down(s8, 8, row)
        win = jnp.where(g == 0, s2, jnp.where(g == 1, s4, jnp.where(g == 2, s8, s16)))
        width = lax.shift_left(jnp.int32(2), g)
        cnt = jnp.minimum(row + 1, width).astype(F32)
        pooled = (win / cnt - u).astype(MX)
        pooled_ref[...] = pooled
        mixed = jnp.dot(pooled, w_ref[0], preferred_element_type=F32)
        mixed_ref[...] = mixed
        scaled_ref[...] = (mixed * s_ref[...]).astype(scaled_ref.dtype)

    return _pcall(body, name=name, grid=(nb, POOL_G),
                  in_specs=[blk, pl.BlockSpec((1, LANE, LANE), lambda b, g: (g, 0, 0)),
                            pl.BlockSpec((1, LANE), lambda b, g: (0, g))],
                  out_specs=[blk, blk, blk],
                  out_shape=[_sds((T, POOL_W), MX), _sds((T, POOL_W), F32), _sds((T, POOL_W), MX)])(z, pool_grp, pool_scale)


def _pool_bwd(dscaled, mixed, pooled, pool_grp, pool_scale, *, nb, name):
    T = dscaled.shape[0]
    S = T // nb
    blk = pl.BlockSpec((S, LANE), lambda g, b: (b, g))

    def body(ds_ref, mixed_ref, pooled_ref, w_ref, s_ref, du_ref, dw_ref, dsc_ref):
        g = pl.program_id(0)
        b = pl.program_id(1)
        ds = ds_ref[...]
        dsc_ref[0] = jnp.sum(ds * mixed_ref[...], axis=0, keepdims=True)
        dmixed = (ds * s_ref[...]).astype(MX)
        dw = lax.dot_general(pooled_ref[...], dmixed, (((0,), (0,)), ((), ())), preferred_element_type=F32)

        @pl.when(b == 0)
        def _():
            dw_ref[0] = dw

        @pl.when(b > 0)
        def _():
            dw_ref[0] += dw
        dpooled = lax.dot_general(dmixed, w_ref[0], (((1,), (1,)), ((), ())), preferred_element_type=F32)
        row = lax.broadcasted_iota(jnp.int32, dpooled.shape, 0)
        width = lax.shift_left(jnp.int32(2), g)
        q = dpooled / jnp.minimum(row + 1, width).astype(F32)
        r2 = q + _shift_up(q, 1, row, S)
        r4 = r2 + _shift_up(r2, 2, row, S)
        r8 = r4 + _shift_up(r4, 4, row, S)
        r16 = r8 + _shift_up(r8, 8, row, S)
        win = jnp.where(g == 0, r2, jnp.where(g == 1, r4, jnp.where(g == 2, r8, r16)))
        du_ref[...] = (win - dpooled).astype(du_ref.dtype)

    return _pcall(body, name=name, grid=(POOL_G, nb),
                  in_specs=[blk, blk, blk, pl.BlockSpec((1, LANE, LANE), lambda g, b: (g, 0, 0)),
                            pl.BlockSpec((1, LANE), lambda g, b: (0, g))],
                  out_specs=[blk, pl.BlockSpec((1, LANE, LANE), lambda g, b: (g, 0, 0)),
                             pl.BlockSpec((1, 1, LANE), lambda g, b: (b, 0, g))],
                  out_shape=[_sds((T, POOL_W), MX), _sds((POOL_G, LANE, LANE), F32), _sds((nb, 1, POOL_W), F32)],
                  )(dscaled, mixed, pooled, pool_grp, pool_scale)


def _lat_fwd(z, q_gain, kv_gain, *, name, tm=256):
    T = z.shape[0]

    def body(z_ref, qg_ref, kg_ref, qn_ref, kvn_ref):
        ql = z_ref[:, Z_QL:Z_QL + QL]
        kv = z_ref[:, Z_KV:Z_KV + KVL]
        qn_ref[...] = (ql * _rsq(ql) * qg_ref[...]).astype(qn_ref.dtype)
        kvn_ref[...] = (kv * _rsq(kv) * kg_ref[...]).astype(kvn_ref.dtype)

    return _pcall(body, name=name, grid=(T // tm,),
                  in_specs=[_row_spec(tm, LAT_W), pl.BlockSpec((1, QL), lambda i: (0, 0)), pl.BlockSpec((1, KVL), lambda i: (0, 0))],
                  out_specs=[_row_spec(tm, QL), _row_spec(tm, KVL)],
                  out_shape=[_sds((T, QL), MX), _sds((T, KVL), MX)])(z, q_gain, kv_gain)


def _lat_bwd(z, dqn, dkvn, q_gain, kv_gain, *, nb, name, tm=256):
    T = z.shape[0]
    tps = (T // nb) // tm

    def norm_bwd(x, dy, gain):
        r = _rsq(x)
        xhat = x * r
        dgain = jnp.sum(dy * xhat, axis=0, keepdims=True)
        dxhat = dy * gain
        return r * (dxhat - xhat * jnp.mean(dxhat * xhat, axis=-1, keepdims=True)), dgain

    def body(z_ref, dq_ref, dkv_ref, qg_ref, kg_ref, dql_ref, dkvl_ref, sq_ref, sk_ref):
        i = pl.program_id(0)
        dql, dqg = norm_bwd(z_ref[:, Z_QL:Z_QL + QL], dq_ref[...], qg_ref[...])
        dkvl, dkg = norm_bwd(z_ref[:, Z_KV:Z_KV + KVL], dkv_ref[...], kg_ref[...])
        dql_ref[...] = dql.astype(dql_ref.dtype)
        dkvl_ref[...] = dkvl.astype(dkvl_ref.dtype)

        @pl.when(i % tps == 0)
        def _():
            sq_ref[...] = jnp.zeros_like(sq_ref)
            sk_ref[...] = jnp.zeros_like(sk_ref)

        sq_ref[0, 0:1, :] += dqg
        sk_ref[0, 0:1, :] += dkg

    return _pcall(body, name=name, grid=(T // tm,),
                  in_specs=[_row_spec(tm, LAT_W), _row_spec(tm, QL), _row_spec(tm, KVL),
                            pl.BlockSpec((1, QL), lambda i: (0, 0)), pl.BlockSpec((1, KVL), lambda i: (0, 0))],
                  out_specs=[_row_spec(tm, QL), _row_spec(tm, KVL),
                             pl.BlockSpec((1, SUBLANE, QL), lambda i: (i // tps, 0, 0)),
                             pl.BlockSpec((1, SUBLANE, KVL), lambda i: (i // tps, 0, 0))],
                  out_shape=[_sds((T, QL), MX), _sds((T, KVL), MX), _sds((nb, SUBLANE, QL), F32), _sds((nb, SUBLANE, KVL), F32)],
                  )(z, dqn, dkvn, q_gain, kv_gain)


def _lane_masks(shape):
    lane = lax.broadcasted_iota(jnp.int32, shape, len(shape) - 1)
    m_n = lane < NOPE
    m_r = jnp.logical_and(lane >= KR_LANE, lane < KR_LANE + ROPE)
    first_half = lane < KR_LANE + ROPE // 2
    return m_n, m_r, first_half


def _rot(y, first_half):
    return jnp.where(first_half, -pltpu.roll(y, LANE - ROPE // 2, 1), pltpu.roll(y, ROPE // 2, 1))


def _rot_t(v, first_half, m_r):
    return jnp.where(m_r, jnp.where(first_half, pltpu.roll(v, LANE - ROPE // 2, 1), -pltpu.roll(v, ROPE // 2, 1)), 0.0)


def _prep_fwd(qp, kvp, z, cos, sin, q_gain, k_gain, *, name, tm=256):
    T = qp.shape[0]
    slab = pl.BlockSpec((tm, LANE), lambda i: (i, 0))
    kr_spec = pl.BlockSpec((tm, LANE), lambda i: (i, Z_KR // LANE))
    vec = pl.BlockSpec((1, LANE), lambda i: (0, 0))

    def body(qp_ref, kvp_ref, kr_ref, cos_ref, sin_ref, qg_ref, kg_ref, q_ref, k_ref, v_ref):
        m_n, m_r, first_half = _lane_masks((tm, LANE))
        c = cos_ref[...]
        s = sin_ref[...]
        qg = qg_ref[...]
        kg = kg_ref[...]
        xr = kr_ref[...]
        rr = lax.rsqrt(jnp.sum(xr * xr, axis=-1, keepdims=True) * (1.0 / ROPE) + EPS)
        yr = xr * rr * kg
        kr = jnp.where(m_r, yr * c + _rot(yr, first_half) * s, 0.0)
        for h in range(NH):
            x = qp_ref[:, h * LANE:(h + 1) * LANE]
            x2 = x * x
            rn = lax.rsqrt(jnp.sum(jnp.where(m_n, x2, 0.0), axis=-1, keepdims=True) * (1.0 / NOPE) + EPS)
            rq = lax.rsqrt(jnp.sum(jnp.where(m_r, x2, 0.0), axis=-1, keepdims=True) * (1.0 / ROPE) + EPS)
            y = x * jnp.where(m_n, rn, jnp.where(m_r, rq, 0.0)) * qg
            q_ref[:, h * LANE:(h + 1) * LANE] = (y * c + _rot(y, first_half) * s).astype(q_ref.dtype)
            xk = kvp_ref[:, h * LANE:(h + 1) * LANE]
            rk = lax.rsqrt(jnp.sum(jnp.where(m_n, xk * xk, 0.0), axis=-1, keepdims=True) * (1.0 / NOPE) + EPS)
            k_ref[:, h * LANE:(h + 1) * LANE] = (jnp.where(m_n, xk * rk * kg, 0.0) + kr).astype(k_ref.dtype)
        v_ref[...] = kvp_ref[:, NH * LANE:].astype(v_ref.dtype)

    return _pcall(body, name=name, grid=(T // tm,),
                  in_specs=[_row_spec(tm, NH * LANE), _row_spec(tm, NH * LANE + NH * NOPE), kr_spec, slab, slab, vec, vec],
                  out_specs=[_row_spec(tm, NH * LANE), _row_spec(tm, NH * LANE), _row_spec(tm, NH * NOPE)],
                  out_shape=[_sds((T, NH * LANE), MX), _sds((T, NH * LANE), MX), _sds((T, NH * NOPE), MX)],
                  )(qp, kvp, z, cos, sin, q_gain, k_gain)


def _prep_bwd(dq, dk, dv, qp, kvp, z, cos, sin, q_gain, k_gain, *, nb, name, tm=256):
    T = qp.shape[0]
    tps = (T // nb) // tm
    slab = pl.BlockSpec((tm, LANE), lambda i: (i, 0))
    kr_spec = pl.BlockSpec((tm, LANE), lambda i: (i, Z_KR // LANE))
    vec = pl.BlockSpec((1, LANE), lambda i: (0, 0))

    def body(dq_ref, dk_ref, dv_ref, qp_ref, kvp_ref, kr_ref, cos_ref, sin_ref, qg_ref, kg_ref,
             dqp_ref, dkvp_ref, dkr_ref, st_ref):
        i = pl.program_id(0)
        m_n, m_r, first_half = _lane_masks((tm, LANE))
        c = cos_ref[...]
        s = sin_ref[...]
        qg = qg_ref[...]
        kg = kg_ref[...]
        dqg = jnp.zeros((1, LANE), F32)
        dkg = jnp.zeros((1, LANE), F32)
        dkr_sum = jnp.zeros((tm, LANE), F32)
        for h in range(NH):
            x = qp_ref[:, h * LANE:(h + 1) * LANE]
            x2 = x * x
            rn = lax.rsqrt(jnp.sum(jnp.where(m_n, x2, 0.0), axis=-1, keepdims=True) * (1.0 / NOPE) + EPS)
            rq = lax.rsqrt(jnp.sum(jnp.where(m_r, x2, 0.0), axis=-1, keepdims=True) * (1.0 / ROPE) + EPS)
            rfac = jnp.where(m_n, rn, jnp.where(m_r, rq, 0.0))
            xhat = x * rfac
            do = dq_ref[:, h * LANE:(h + 1) * LANE]
            dy = do * c + _rot_t(do * s, first_half, m_r)
            dqg = dqg + jnp.sum(dy * xhat, axis=0, keepdims=True)
            dxhat = dy * qg
            t = dxhat * xhat
            mean_n = jnp.sum(jnp.where(m_n, t, 0.0), axis=-1, keepdims=True) * (1.0 / NOPE)
            mean_r = jnp.sum(jnp.where(m_r, t, 0.0), axis=-1, keepdims=True) * (1.0 / ROPE)
            dqp_ref[:, h * LANE:(h + 1) * LANE] = (
                rfac * (dxhat - xhat * jnp.where(m_n, mean_n, jnp.where(m_r, mean_r, 0.0)))).astype(dqp_ref.dtype)

            xk = kvp_ref[:, h * LANE:(h + 1) * LANE]
            rk = lax.rsqrt(jnp.sum(jnp.where(m_n, xk * xk, 0.0), axis=-1, keepdims=True) * (1.0 / NOPE) + EPS)
            khat = jnp.where(m_n, xk * rk, 0.0)
            dko = dk_ref[:, h * LANE:(h + 1) * LANE]
            dkn = jnp.where(m_n, dko, 0.0)
            dkg = dkg + jnp.sum(dkn * khat, axis=0, keepdims=True)
            dkhat = dkn * kg
            mean_k = jnp.sum(dkhat * khat, axis=-1, keepdims=True) * (1.0 / NOPE)
            dkvp_ref[:, h * LANE:(h + 1) * LANE] = jnp.where(m_n, rk * (dkhat - khat * mean_k), 0.0).astype(dkvp_ref.dtype)
            dkr_sum = dkr_sum + jnp.where(m_r, dko, 0.0)
        dkvp_ref[:, NH * LANE:] = dv_ref[...].astype(dkvp_ref.dtype)

        xr = kr_ref[...]
        rr = lax.rsqrt(jnp.sum(xr * xr, axis=-1, keepdims=True) * (1.0 / ROPE) + EPS)
        rhat = xr * rr
        dyr = dkr_sum * c + _rot_t(dkr_sum * s, first_half, m_r)
        dkg = dkg + jnp.sum(dyr * rhat, axis=0, keepdims=True)
        drhat = dyr * kg
        mean_kr = jnp.sum(drhat * rhat, axis=-1, keepdims=True) * (1.0 / ROPE)
        dkr_ref[...] = jnp.where(m_r, rr * (drhat - rhat * mean_kr), 0.0).astype(dkr_ref.dtype)

        @pl.when(i % tps == 0)
        def _():
            st_ref[...] = jnp.zeros_like(st_ref)

        st_ref[0, 0:1, :] += dqg
        st_ref[0, 1:2, :] += dkg

    return _pcall(body, name=name, grid=(T // tm,),
                  in_specs=[_row_spec(tm, NH * LANE), _row_spec(tm, NH * LANE), _row_spec(tm, NH * NOPE),
                            _row_spec(tm, NH * LANE), _row_spec(tm, NH * LANE + NH * NOPE), kr_spec, slab, slab, vec, vec],
                  out_specs=[_row_spec(tm, NH * LANE), _row_spec(tm, NH * LANE + NH * NOPE), slab,
                             pl.BlockSpec((1, SUBLANE, LANE), lambda i: (i // tps, 0, 0))],
                  out_shape=[_sds((T, NH * LANE), MX), _sds((T, NH * LANE + NH * NOPE), MX), _sds((T, LANE), MX),
                             _sds((nb, SUBLANE, LANE), F32)],
                  )(dq, dk, dv, qp, kvp, z, cos, sin, q_gain, k_gain)


def _lower_triangle(t):
    return lax.broadcasted_iota(jnp.int32, (t, t), 1) <= lax.broadcasted_iota(jnp.int32, (t, t), 0)


def _attn_fwd(q, k, v, *, nb, name, tq=512, comm=None):
    T = q.shape[0]
    S = T // nb
    tq = _pick(S, tq)
    nq = S // tq
    tk = tq
    npair = NH // 2

    def body(q_ref, k_ref, v_ref, o_ref, lse_ref):
        qi = pl.program_id(2)
        lane = lax.broadcasted_iota(jnp.int32, (tq, LANE), 1)
        qs = [q_ref[:, hh * LANE:(hh + 1) * LANE] for hh in range(2)]

        def block(j, carry, diagonal):
            k0 = pl.multiple_of(j * tk, tk)
            vb = v_ref[pl.ds(k0, tk), :]
            new = []
            for hh in range(2):
                m, l, acc = carry[hh]
                kb = k_ref[pl.ds(k0, tk), hh * LANE:(hh + 1) * LANE]
                s = lax.dot_general(qs[hh], kb, (((1,), (1,)), ((), ())), preferred_element_type=F32) * ATTN_SCALE
                if diagonal:
                    s = jnp.where(_lower_triangle(tq), s, NEG)
                m_new = jnp.maximum(m, jnp.max(s, axis=-1, keepdims=True))
                p = jnp.exp(s - m_new)
                alpha = jnp.exp(m - m_new)
                l = alpha * l + jnp.sum(p, axis=-1, keepdims=True)
                acc = alpha * acc + jnp.dot(p.astype(MX), vb, preferred_element_type=F32)
                new.append((m_new, l, acc))
            return tuple(new)

        init = tuple((jnp.full((tq, 1), NEG, F32), jnp.zeros((tq, 1), F32), jnp.zeros((tq, LANE), F32)) for _ in range(2))
        carry = lax.fori_loop(0, qi, lambda j, c: block(j, c, False), init)
        (m0, l0, acc0), (m1, l1, acc1) = block(qi, carry, True)
        o_ref[...] = jnp.where(lane < NOPE, acc0 / l0, acc1 / l1).astype(o_ref.dtype)
        lse_ref[...] = jnp.where(lane < NOPE, m0 + jnp.log(l0), m1 + jnp.log(l1))

    return _pcall(body, name=name, grid=(nb, npair, nq),
                  in_specs=[pl.BlockSpec((tq, 2 * LANE), lambda b, p, i: (b * nq + i, p)),
                            pl.BlockSpec((S, 2 * LANE), lambda b, p, i: (b, p)),
                            pl.BlockSpec((S, LANE), lambda b, p, i: (b, p))],
                  out_specs=[pl.BlockSpec((tq, LANE), lambda b, p, i: (b * nq + i, p)),
                             pl.BlockSpec((tq, LANE), lambda b, p, i: (b * nq + i, p))],
                  out_shape=[_sds((T, NH * NOPE), MX), _sds((T, NH * NOPE), F32)], comm=comm)(q, k, v)


def _attn_bwd(q, k, v, o, lse, do, *, nb, name, tq=512, comm=None):
    T = q.shape[0]
    S = T // nb
    tq = _pick(S, tq)
    nq = S // tq
    tk = tq
    npair = NH // 2

    def body(q_ref, k_ref, v_ref, o_ref, lse_ref, do_ref, dq_ref, dk_ref, dv_ref, delta_ref):
        lane = lax.broadcasted_iota(jnp.int32, (tq, LANE), 1)
        first = lane < NOPE
        dq_ref[...] = jnp.zeros_like(dq_ref)

        def delta_step(qi, _):
            q0 = pl.multiple_of(qi * tq, tq)
            prod = do_ref[pl.ds(q0, tq), :] * o_ref[pl.ds(q0, tq), :].astype(F32)
            d0 = jnp.sum(jnp.where(first, prod, 0.0), axis=-1, keepdims=True)
            d1 = jnp.sum(jnp.where(first, 0.0, prod), axis=-1, keepdims=True)
            delta_ref[pl.ds(q0, tq), :] = jnp.where(first, d0, d1)
            return 0

        lax.fori_loop(0, nq, delta_step, 0)

        def kv_step(kj, _):
            k0 = pl.multiple_of(kj * tk, tk)
            kbs = [k_ref[pl.ds(k0, tk), hh * LANE:(hh + 1) * LANE] for hh in range(2)]
            vb = v_ref[pl.ds(k0, tk), :]

            def q_block(qi, carry, diagonal):
                dk0, dk1, dv = carry
                dks = [dk0, dk1]
                q0 = pl.multiple_of(qi * tq, tq)
                dov = do_ref[pl.ds(q0, tq), :]
                lse_v = lse_ref[pl.ds(q0, tq), :]
                delta_v = delta_ref[pl.ds(q0, tq), :]
                for hh in range(2):
                    qh = q_ref[pl.ds(q0, tq), hh * LANE:(hh + 1) * LANE]
                    dob = jnp.where(first if hh == 0 else jnp.logical_not(first), dov, 0.0).astype(MX)
                    s = lax.dot_general(qh, kbs[hh], (((1,), (1,)), ((), ())), preferred_element_type=F32) * ATTN_SCALE
                    p = jnp.exp(s - lse_v[:, hh * NOPE:hh * NOPE + 1])
                    if diagonal:
                        p = jnp.where(_lower_triangle(tq), p, 0.0)
                    dp = lax.dot_general(dob, vb, (((1,), (1,)), ((), ())), preferred_element_type=F32)
                    ds = (p * (dp - delta_v[:, hh * NOPE:hh * NOPE + 1]) * ATTN_SCALE).astype(MX)
                    dks[hh] = dks[hh] + lax.dot_general(ds, qh, (((0,), (0,)), ((), ())), preferred_element_type=F32)
                    dv = dv + lax.dot_general(p.astype(MX), dob, (((0,), (0,)), ((), ())), preferred_element_type=F32)
                    dq_ref[pl.ds(q0, tq), hh * LANE:(hh + 1) * LANE] += jnp.dot(ds, kbs[hh], preferred_element_type=F32)
                return dks[0], dks[1], dv

            zero = jnp.zeros((tk, LANE), F32)
            carry = q_block(kj, (zero, zero, zero), True)
            dk0, dk1, dv = lax.fori_loop(kj + 1, nq, lambda qi, c: q_block(qi, c, False), carry)
            dk_ref[pl.ds(k0, tk), 0:LANE] = dk0
            dk_ref[pl.ds(k0, tk), LANE:2 * LANE] = dk1
            dv_ref[pl.ds(k0, tk), :] = dv
            return 0

        lax.fori_loop(0, nq, kv_step, 0)

    pair256 = pl.BlockSpec((S, 2 * LANE), lambda b, p: (b, p))
    pair128 = pl.BlockSpec((S, LANE), lambda b, p: (b, p))
    return _pcall(body, name=name, grid=(nb, npair),
                  in_specs=[pair256, pair256, pair128, pair128, pair128, pair128],
                  out_specs=[pair256, pair256, pair128],
                  out_shape=[_sds((T, NH * LANE), F32), _sds((T, NH * LANE), F32), _sds((T, NH * NOPE), F32)],
                  scratch=[pltpu.VMEM((S, LANE), F32)], comm=comm)(q, k, v, o, lse, do)


class _NoExchange:
    def __init__(self, weights):
        self.W, self.G = weights, {}

    def rider(self, name):
        return None

    def landed(self, name, outs):
        pass

    def grad(self, key, g):
        self.G[key] = g


def _run(plan, fn, *args, name, **kw):
    rider = plan.rider(name)
    if rider is None:
        return fn(*args, name=name, **kw)
    outs, landed = fn(*args, name=name, comm=rider, **kw)
    plan.landed(name, landed)
    return outs


def _layer_fwd_bwd(x, tgt, mod3, cos, sin, plan, P):
    nb = mod3.shape[0]
    W = plan.W
    h1, gu1, a1 = _run(plan, _ffn_up_first, x, mod3, P["norm_ffn1"], W["ffn1_in_lo"], sub=0, name="ffn1_up_a")
    gu1, a1 = _run(plan, _ffn_up_second, h1, W["ffn1_in_hi"], gu1, a1, name="ffn1_up_b")
    f1, x1, h2 = _run(plan, _mm_resid_norm, a1, W["ffn1_out"], x, mod3, P["norm_mix"], sub=1, coef=0.5, name="ffn1_out")
    z = _run(plan, _mm, h2, W["w_in"], mode="nn", out_dtype=F32, name="mix_in", tn=1664)
    pooled, mixed, scaled = _pool_fwd(z, W["pool_grp"], P["pool_scale"], nb=nb, name="pool_fwd")
    br_pool = _mm(scaled, W["pool_proj"], mode="nn", out_dtype=F32, name="pool_proj")
    qn, kvn = _lat_fwd(z, P["q_a_norm"], P["kv_a_norm"], name="lat_fwd")
    qp = _mm(qn, W["q_up"], mode="nn", out_dtype=F32, name="q_up")
    kvp = _mm(kvn, W["kv_up"], mode="nn", out_dtype=F32, name="kv_up")
    q, k, v = _prep_fwd(qp, kvp, z, cos, sin, P["q_gain"], P["k_gain"], name="prep_fwd")
    attn, lse = _run(plan, _attn_fwd, q, k, v, nb=nb, name="attn_fwd")
    br_mla = _mm(attn, W["mla_proj"], mode="nn", out_dtype=F32, name="mla_proj")
    merged = _merge_fwd(z, br_pool, br_mla, name="merge_fwd")
    mo, x2, h3 = _mm_resid_norm(merged, W["w_out"], x1, mod3, P["norm_ffn2"], sub=2, coef=1.0, name="mix_out")
    gu2, a2 = _ffn_up(h3, W["ffn2_in"], name="ffn2_up")
    dy, df2, st_fin, loss = _mm_loss(a2, W["ffn2_out"], x2, tgt, mod3, name="ffn2_out_loss")

    plan.grad("ffn2_out", _dw(a2, df2, name="d_ffn2_out"))
    dgu2 = _ffn_dact(df2, gu2, W["ffn2_out"], name="ffn2_dact")
    plan.grad("ffn2_in", _ffn_dw_in(h3, dgu2, name="d_ffn2_in"))
    dx2, dmo, st3 = _run(plan, _ffn_dh_norm, dgu2, W["ffn2_in"], x2, dy, mo, mod3, P["norm_ffn2"], sub=2, coef=1.0,
                         name="d_ffn2_h")
    plan.grad("w_out", _dw(merged, dmo, name="d_mix_out"))
    dmerged = _mm(dmo, W["w_out"], mode="nt", out_dtype=F32, name="d_merged")
    dbr_pool, dbr_mla, dgates = _merge_bwd(z, br_pool, br_mla, dmerged, name="merge_bwd")
    plan.grad("pool_proj", _dw(scaled, dbr_pool, name="d_pool_proj"))
    dscaled = _mm(dbr_pool, W["pool_proj"], mode="nt", out_dtype=F32, name="d_pool_scaled")
    du_pool, d_pool_grp, d_pool_scale = _pool_bwd(dscaled, mixed, pooled, W["pool_grp"], P["pool_scale"], nb=nb, name="pool_bwd")
    plan.grad("mla_proj", _dw(attn, dbr_mla, name="d_mla_proj"))
    dattn = _mm(dbr_mla, W["mla_proj"], mode="nt", out_dtype=F32, name="d_attn")
    dq, dk, dv = _run(plan, _attn_bwd, q, k, v, attn, lse, dattn, nb=nb, name="attn_bwd")
    dqp, dkvp, dkr, st_prep = _prep_bwd(dq, dk, dv, qp, kvp, z, cos, sin, P["q_gain"], P["k_gain"], nb=nb, name="prep_bwd")
    plan.grad("q_up", _dw(qn, dqp, name="d_q_up"))
    dqn = _mm(dqp, W["q_up"], mode="nt", out_dtype=F32, name="d_qn")
    plan.grad("kv_up", _dw(kvn, dkvp, name="d_kv_up"))
    dkvn = _mm(dkvp, W["kv_up"], mode="nt", out_dtype=F32, name="d_kvn")
    dql, dkvl, st_q, st_kv = _lat_bwd(z, dqn, dkvn, P["q_a_norm"], P["kv_a_norm"], nb=nb, name="lat_bwd")
    dz = jnp.concatenate([du_pool, dql, dkvl, dkr, dgates], axis=1)
    plan.grad("w_in", _run(plan, _dw, h2, dz, name="d_mix_in", tn=1664))
    dx1, df1, st2 = _run(plan, _mm_nt_norm, dz, W["w_in"], x1, dx2, f1, mod3, P["norm_mix"], sub=1, coef=0.5, name="d_mix_h")
    plan.grad("ffn1_out", _run(plan, _dw, a1, df1, name="d_ffn1_out"))
    dgu1 = _run(plan, _ffn_dact, df1, gu1, W["ffn1_out"], name="ffn1_dact")
    half = x.shape[1] // 2
    plan.grad("ffn1_in@0", _run(plan, _ffn_dw_in, h1, dgu1, rows=(0, half), name="d_ffn1_in_a"))
    plan.grad("ffn1_in@1", _run(plan, _ffn_dw_in, h1, dgu1, rows=(half, half), name="d_ffn1_in_b"))
    dh1 = _run(plan, _ffn_dh, dgu1, W["ffn1_in_lo"], W["ffn1_in_hi"], name="d_ffn1_h")
    grad_x, st1 = _run(plan, _bwd_block, x, dh1, dx1, None, mod3, P["norm_ffn1"], sub=0, coef=0.0, name="bwd_norm1")

    return loss, grad_x, (st1, st2, st3, st_fin, st_q, st_kv, st_prep, d_pool_scale), d_pool_grp


def _w_in_to_kernel(w):
    k = w.shape[0]
    zeros = lambda n: jnp.zeros((k, n), w.dtype)
    return jnp.concatenate([w[:, 0:1152], zeros(KR_LANE), w[:, 1152:1184], zeros(LANE - KR_LANE - ROPE), w[:, 1184:]], axis=1)


def _w_in_from_kernel(g):
    return jnp.concatenate([g[:, 0:1152], g[:, Z_KR + KR_LANE:Z_KR + KR_LANE + ROPE], g[:, Z_GP:]], axis=1)


def _q_up_to_kernel(w):
    k = w.shape[0]
    return jnp.pad(w.reshape(k, NH, NOPE + ROPE), ((0, 0), (0, 0), (0, LANE - NOPE - ROPE))).reshape(k, NH * LANE)


def _q_up_from_kernel(g):
    k = g.shape[0]
    return g.reshape(k, NH, LANE)[:, :, :NOPE + ROPE].reshape(k, NH * (NOPE + ROPE))


def _kv_up_to_kernel(w):
    k = w.shape[0]
    w3 = w.reshape(k, NH, 2 * NOPE)
    kpart = jnp.pad(w3[:, :, :NOPE], ((0, 0), (0, 0), (0, LANE - NOPE))).reshape(k, NH * LANE)
    return jnp.concatenate([kpart, w3[:, :, NOPE:].reshape(k, NH * NOPE)], axis=1)


def _kv_up_from_kernel(g):
    k = g.shape[0]
    kpart = g[:, :NH * LANE].reshape(k, NH, LANE)[:, :, :NOPE]
    vpart = g[:, NH * LANE:].reshape(k, NH, NOPE)
    return jnp.concatenate([kpart, vpart], axis=2).reshape(k, NH * 2 * NOPE)


def _gain_slab(nope, rope):
    return jnp.concatenate([nope, rope, jnp.zeros((1, LANE - NOPE - ROPE), nope.dtype)], axis=1)


def _rope_tables(positions):
    inv_freq = 10000.0 ** (-jnp.arange(0, ROPE, 2, dtype=F32) / ROPE)
    ang = positions.astype(F32)[:, None] * inv_freq
    ang = jnp.concatenate([ang, ang], axis=-1)
    t = positions.shape[0]
    cos = jnp.concatenate([jnp.ones((t, KR_LANE), F32), jnp.cos(ang), jnp.ones((t, LANE - KR_LANE - ROPE), F32)], axis=1)
    sin = jnp.concatenate([jnp.zeros((t, KR_LANE), F32), jnp.sin(ang), jnp.zeros((t, LANE - KR_LANE - ROPE), F32)], axis=1)
    return cos, sin


def _coords():
    return lax.axis_index("x"), lax.axis_index("y"), lax.axis_index("c")


HBM_SPEC = pl.BlockSpec(memory_space=pl.ANY)
VMEM_SPEC = pl.BlockSpec(memory_space=pltpu.VMEM)


def _gather8(v, *, name, comm=None):
    r, c = v.shape

    def body(v_ref, out_ref, send_sems, recv_sems, local_sem):
        x, y, cc = _coords()
        me = 4 * x + 2 * y + cc
        mine = pltpu.make_async_copy(v_ref, out_ref.at[me], local_sem)
        mine.start()
        copies = []
        for kk in range(1, N_DEV):
            peer = (x ^ (kk >> 2), y ^ ((kk >> 1) & 1), cc ^ (kk & 1))
            cp = pltpu.make_async_remote_copy(src_ref=v_ref, dst_ref=out_ref.at[me], send_sem=send_sems.at[kk - 1],
                                              recv_sem=recv_sems.at[kk - 1], device_id=peer, device_id_type=MESH)
            cp.start()
            copies.append(cp)
        for kk in range(1, N_DEV):
            peer_slot = me ^ kk
            pltpu.make_async_remote_copy(src_ref=v_ref, dst_ref=out_ref.at[peer_slot], send_sem=send_sems.at[kk - 1],
                                         recv_sem=recv_sems.at[kk - 1], device_id=(x, y, cc), device_id_type=MESH).wait_recv()
        for cp in copies:
            cp.wait_send()
        mine.wait()

    return _pcall(body, name=name, out_shape=_sds((N_DEV, r, c), v.dtype), in_specs=[VMEM_SPEC], out_specs=VMEM_SPEC,
                  scratch=[pltpu.SemaphoreType.DMA((N_DEV - 1,)), pltpu.SemaphoreType.DMA((N_DEV - 1,)), pltpu.SemaphoreType.DMA],
                  comm=comm)(v)


CHIP_RELS = ((1, 0), (0, 1), (1, 1))


def _comm_only(comm, *, name):
    _, outs = _pcall(lambda: None, name=name, out_shape=[], in_specs=[], out_specs=[], comm=comm)()
    return outs


def _gather_comm(shards, only_y=None):
    n = len(shards)

    def sends(y):
        return True if only_y is None else y == only_y

    def if_(cond, fn):
        if cond is True:
            fn()
        else:
            pl.when(cond)(fn)

    def ici(ins, outs, sems, a, j, x, y, cc):
        half = ins[a].shape[0] // 2
        mine = pl.ds(cc * half, half)
        dx, dy = CHIP_RELS[j]
        return pltpu.make_async_remote_copy(src_ref=ins[a].at[mine], dst_ref=outs[a].at[2 * x + y, mine],
                                            send_sem=sems[0].at[a, j], recv_sem=sems[1].at[a, j],
                                            device_id=(x ^ dx, y ^ dy, cc), device_id_type=MESH)

    def d2d(ins, outs, sems, a, j, x, y, cc, half_of):
        half = ins[a].shape[0] // 2
        dx, dy = CHIP_RELS[j]
        landed = outs[a].at[2 * (x ^ dx) + (y ^ dy), pl.ds(half_of * half, half)]
        return pltpu.make_async_remote_copy(src_ref=landed, dst_ref=landed, send_sem=sems[2].at[a, j], recv_sem=sems[3].at[a, j],
                                            device_id=(x, y, 1 - cc), device_id_type=MESH)

    def own(ins, outs, sems, a, x, y):
        return pltpu.make_async_copy(ins[a], outs[a].at[2 * x + y], sems[4].at[a])

    def start(ins, outs, sems):
        x, y, cc = _coords()

        def go():
            for a in range(n):
                own(ins, outs, sems, a, x, y).start()
                for j in range(3):
                    ici(ins, outs, sems, a, j, x, y, cc).start()

        if_(sends(y), go)

    def finish(ins, outs, sems):
        x, y, cc = _coords()
        for j, (dx, dy) in enumerate(CHIP_RELS):
            def land_and_pass(j=j):
                for a in range(n):
                    ici(ins, outs, sems, a, j, x, y, cc).wait_recv()
                    d2d(ins, outs, sems, a, j, x, y, cc, cc).start()

            if_(sends(y ^ dy), land_and_pass)
        for j, (dx, dy) in enumerate(CHIP_RELS):
            def from_sibling(j=j):
                for a in range(n):
                    d2d(ins, outs, sems, a, j, x, y, cc, 1 - cc).wait_recv()

            if_(sends(y ^ dy), from_sibling)
        for j, (dx, dy) in enumerate(CHIP_RELS):
            def passed(j=j):
                for a in range(n):
                    d2d(ins, outs, sems, a, j, x, y, cc, cc).wait_send()

            if_(sends(y ^ dy), passed)

        def sent():
            for a in range(n):
                for j in range(3):
                    ici(ins, outs, sems, a, j, x, y, cc).wait_send()
                own(ins, outs, sems, a, x, y).wait()

        if_(sends(y), sent)

    dma = pltpu.SemaphoreType.DMA
    return _Comm(shards, [_sds((N_CHIP,) + s.shape, s.dtype) for s in shards],
                 [dma((n, 3)), dma((n, 3)), dma((n, 3)), dma((n, 3)), dma((n,))], start, finish)


def _swap_comm(parts):
    n = len(parts)

    def copy(ins, outs, sems, a):
        x, y, cc = _coords()
        half = ins[a].shape[1] // 2
        return pltpu.make_async_remote_copy(src_ref=ins[a].at[:, pl.ds((1 - cc) * half, half)], dst_ref=outs[a],
                                            send_sem=sems[0].at[a], recv_sem=sems[1].at[a], device_id=(x, y, 1 - cc),
                                            device_id_type=MESH)

    def start(ins, outs, sems):
        for a in range(n):
            copy(ins, outs, sems, a).start()

    def finish(ins, outs, sems):
        for a in range(n):
            copy(ins, outs, sems, a).wait()

    dma = pltpu.SemaphoreType.DMA
    return _Comm(parts, [_sds((p.shape[0], p.shape[1] // 2, p.shape[2]), p.dtype) for p in parts], [dma((n,)), dma((n,))],
                 start, finish)


def _add_half(full, other, cidx, *, name):
    nch, r, c = full.shape
    half = r // 2
    tr = _pick_rows(half)
    nbk = half // tr
    grid_spec = pltpu.PrefetchScalarGridSpec(
        num_scalar_prefetch=1, grid=(nch, nbk),
        in_specs=[pl.BlockSpec((1, tr, c), lambda j, i, cref: (j, cref[0] * nbk + i, 0)),
                  pl.BlockSpec((1, tr, c), lambda j, i, cref: (j, i, 0))],
        out_specs=pl.BlockSpec((1, tr, c), lambda j, i, cref: (j, i, 0)))

    def body(cref, a_ref, b_ref, o_ref):
        o_ref[...] = (a_ref[...] + b_ref[...]).astype(o_ref.dtype)

    return _pcall(body, name=name, out_shape=_sds((nch, half, c), MX), grid_spec=grid_spec)(cidx, full, other)


def _pick_rows(rows, target=512):
    best = None
    for t in range(16, min(rows, target) + 1, 16):
        if rows % t == 0:
            best = t
    return rows if best is None else best


def _exchange_comm(parts):
    n = len(parts)

    def send(ins, outs, sems, a, j, x, y, cc):
        dx, dy = CHIP_RELS[j]
        return pltpu.make_async_remote_copy(src_ref=ins[a].at[2 * (x ^ dx) + (y ^ dy)], dst_ref=outs[a].at[2 * x + y],
                                            send_sem=sems[0].at[a, j], recv_sem=sems[1].at[a, j],
                                            device_id=(x ^ dx, y ^ dy, cc), device_id_type=MESH)

    def landing(ins, outs, sems, a, j, x, y, cc):
        dx, dy = CHIP_RELS[j]
        peer_chip = 2 * (x ^ dx) + (y ^ dy)
        return pltpu.make_async_remote_copy(src_ref=ins[a].at[peer_chip], dst_ref=outs[a].at[peer_chip], send_sem=sems[0].at[a, j],
                                            recv_sem=sems[1].at[a, j], device_id=(x, y, cc), device_id_type=MESH)

    def own(ins, outs, sems, a, x, y):
        return pltpu.make_async_copy(ins[a].at[2 * x + y], outs[a].at[2 * x + y], sems[2].at[a])

    def start(ins, outs, sems):
        x, y, cc = _coords()
        for a in range(n):
            own(ins, outs, sems, a, x, y).start()
            for j in range(3):
                send(ins, outs, sems, a, j, x, y, cc).start()

    def finish(ins, outs, sems):
        x, y, cc = _coords()
        for a in range(n):
            for j in range(3):
                landing(ins, outs, sems, a, j, x, y, cc).wait_recv()
        for a in range(n):
            for j in range(3):
                send(ins, outs, sems, a, j, x, y, cc).wait_send()
            own(ins, outs, sems, a, x, y).wait()

    dma = pltpu.SemaphoreType.DMA
    return _Comm(parts, [_sds(p.shape, p.dtype) for p in parts], [dma((n, 3)), dma((n, 3)), dma((n,))], start, finish)


def _sum_chips(q, cidx, *, name):
    nch, h, c = q.shape
    tr = _pick_rows(h)
    nbk = h // tr
    grid_spec = pltpu.PrefetchScalarGridSpec(
        num_scalar_prefetch=1, grid=(nbk,),
        in_specs=[pl.BlockSpec((nch, tr, c), lambda i, cref: (0, i, 0))],
        out_specs=pl.BlockSpec((tr, c), lambda i, cref: (cref[0] * nbk + i, 0)))

    def body(cref, q_ref, o_ref):
        acc = q_ref[0].astype(F32) + q_ref[1].astype(F32)
        acc = acc + q_ref[2].astype(F32)
        o_ref[...] = acc + q_ref[3].astype(F32)

    return _pcall(body, name=name, out_shape=_sds((2 * h, c), F32), grid_spec=grid_spec)(cidx, q)


def _join_comm(fulls):
    n = len(fulls)

    def half_copy(outs, sems, a, which):
        x, y, cc = _coords()
        h = outs[a].shape[0] // 2
        rows = outs[a].at[pl.ds((cc if which == 0 else 1 - cc) * h, h)]
        return pltpu.make_async_remote_copy(src_ref=rows, dst_ref=rows, send_sem=sems[0].at[a], recv_sem=sems[1].at[a],
                                            device_id=(x, y, 1 - cc), device_id_type=MESH)

    def start(ins, outs, sems):
        for a in range(n):
            half_copy(outs, sems, a, 0).start()

    def finish(ins, outs, sems):
        for a in range(n):
            half_copy(outs, sems, a, 1).wait_recv()
        for a in range(n):
            half_copy(outs, sems, a, 0).wait_send()

    dma = pltpu.SemaphoreType.DMA
    return _Comm(fulls, [_sds(p.shape, p.dtype) for p in fulls], [dma((n,)), dma((n,))], start, finish,
                 aliases={a: a for a in range(n)})


def _ada_prologue(c, w, b, *, name, comm=None):
    nb, dm = c.shape
    n = w.shape[1]

    def body(c_ref, w_ref, b_ref, call_ref, land_ref, cpad, mod_s, g_send, g_recv, m_send, m_recv):
        x, y, cc = _coords()
        me = 4 * x + 2 * y + cc
        chip = 2 * x + y
        cpad[...] = jnp.zeros_like(cpad)
        cpad[0:nb, :] = c_ref[...]
        copies = []
        for kk in range(1, N_DEV):
            peer = (x ^ (kk >> 2), y ^ ((kk >> 1) & 1), cc ^ (kk & 1))
            cp = pltpu.make_async_remote_copy(src_ref=cpad, dst_ref=call_ref.at[me], send_sem=g_send.at[kk - 1],
                                              recv_sem=g_recv.at[kk - 1], device_id=peer, device_id_type=MESH)
            cp.start()
            copies.append(cp)
        call_ref[me] = cpad[...]
        for kk in range(1, N_DEV):
            pltpu.make_async_remote_copy(src_ref=cpad, dst_ref=call_ref.at[me ^ kk], send_sem=g_send.at[kk - 1],
                                         recv_sem=g_recv.at[kk - 1], device_id=(x, y, cc), device_id_type=MESH).wait_recv()
        cv = call_ref[...].reshape(N_DEV * SUBLANE, dm)
        act = (cv * _sigmoid(cv)).astype(MX)
        mod = jnp.dot(act, w_ref[...].astype(MX), preferred_element_type=F32) + b_ref[...]
        mod_s[...] = mod.reshape(N_DEV, SUBLANE, n)
        for j, (dx, dy) in enumerate(CHIP_RELS):
            cp = pltpu.make_async_remote_copy(src_ref=mod_s.at[4 * (x ^ dx) + 2 * (y ^ dy) + cc], dst_ref=land_ref.at[chip],
                                              send_sem=m_send.at[j], recv_sem=m_recv.at[j],
                                              device_id=(x ^ dx, y ^ dy, cc), device_id_type=MESH)
            cp.start()
            copies.append(cp)
        land_ref[chip] = mod_s[me]
        for j, (dx, dy) in enumerate(CHIP_RELS):
            pltpu.make_async_remote_copy(src_ref=mod_s.at[me], dst_ref=land_ref.at[2 * (x ^ dx) + (y ^ dy)], send_sem=m_send.at[j],
                                         recv_sem=m_recv.at[j], device_id=(x, y, cc), device_id_type=MESH).wait_recv()
        for cp in copies:
            cp.wait_send()

    dma = pltpu.SemaphoreType.DMA
    return _pcall(body, name=name, in_specs=[VMEM_SPEC] * 3, out_specs=[VMEM_SPEC] * 2,
                  out_shape=[_sds((N_DEV, SUBLANE, dm), F32), _sds((N_CHIP, SUBLANE, n), F32)],
                  scratch=[pltpu.VMEM((SUBLANE, dm), F32), pltpu.VMEM((N_DEV, SUBLANE, n), F32),
                           dma((N_DEV - 1,)), dma((N_DEV - 1,)), dma((3,)), dma((3,))], comm=comm)(c, w, b)


def _ada_bwd(c_all, dmod_cols, *, name):
    m, kdim = c_all.shape
    n = dmod_cols.shape[1]
    tn = _pick(n, 1152)

    def body(c_ref, d_ref, o_ref):
        cv = c_ref[...]
        act = (cv * _sigmoid(cv)).astype(MX)
        o_ref[...] = lax.dot_general(act, d_ref[...].astype(MX), (((0,), (0,)), ((), ())), preferred_element_type=F32)

    return _pcall(body, name=name, out_shape=_sds((kdim, n), F32), grid=(n // tn,),
                  in_specs=[pl.BlockSpec((m, kdim), lambda j: (0, 0)), pl.BlockSpec((m, tn), lambda j: (0, j))],
                  out_specs=pl.BlockSpec((kdim, tn), lambda j: (0, j)))(c_all, dmod_cols)


SLAB_W = 1024
RED_ROWS = 8


LOSS_LANE = SLAB_W - LANE


def _pack_stats(st1, st2, st3, st_fin, st_q, st_kv, st_prep, d_pool_scale, loss8, *, name):
    nb = st1.shape[0]
    dm_rows = -(-9 * nb // SUBLANE) * SUBLANE

    def body(s1, s2, s3, sf, sq, skv, sp, sps, loss_ref, o_ref):
        o_ref[...] = jnp.zeros_like(o_ref)
        for s in range(nb):
            rows = [s1[s, 0:1, :], s1[s, 1:2, :], s2[s, 3:4, :], s2[s, 0:1, :], s2[s, 1:2, :], s3[s, 3:4, :], s3[s, 0:1, :],
                    s3[s, 1:2, :], sf[s, 0:1, :]]
            for k, row in enumerate(rows):
                o_ref[9 * s + k:9 * s + k + 1, :] = row

        def over_seq(ref, r):
            acc = ref[0, r:r + 1, :]
            for s in range(1, nb):
                acc = acc + ref[s, r:r + 1, :]
            return acc

        o_ref[dm_rows + 0:dm_rows + 1, :] = over_seq(s1, 2)
        o_ref[dm_rows + 1:dm_rows + 2, :] = over_seq(s2, 2)
        o_ref[dm_rows + 2:dm_rows + 3, :] = over_seq(s3, 2)
        o_ref[dm_rows + 3:dm_rows + 4, 0:POOL_W] = over_seq(sps, 0)
        o_ref[dm_rows + 4:dm_rows + 5, 0:QL] = over_seq(sq, 0)
        o_ref[dm_rows + 5:dm_rows + 6, 0:KVL] = over_seq(skv, 0)
        o_ref[dm_rows + 6:dm_rows + 7, 0:LANE] = over_seq(sp, 0)
        o_ref[dm_rows + 7:dm_rows + 8, 0:LANE] = over_seq(sp, 1)
        o_ref[dm_rows + 7:dm_rows + 8, LOSS_LANE:] = loss_ref[0:1, :]

    return _pcall(body, name=name, out_shape=_sds((dm_rows + RED_ROWS, SLAB_W), F32), in_specs=[VMEM_SPEC] * 9,
                  out_specs=VMEM_SPEC)(st1, st2, st3, st_fin, st_q, st_kv, st_prep, d_pool_scale, loss8)


def _small_allreduce(slab, pool, *, name, comm=None):
    rows, w = slab.shape
    dm = rows - RED_ROWS
    prow, pw = pool.shape
    crow = RED_ROWS + 2 * dm

    def body(slab_ref, pool_ref, red_ref, dmod_ref, ptot_ref, sib_slab, sib_pool, chip_slab, chip_pool, land_slab, land_pool,
             a_send, a_recv, b_send, b_recv):
        x, y, cc = _coords()
        chip = 2 * x + y
        sib = (x, y, 1 - cc)
        to_sib = [pltpu.make_async_remote_copy(src_ref=slab_ref, dst_ref=sib_slab, send_sem=a_send.at[0], recv_sem=a_recv.at[0],
                                               device_id=sib, device_id_type=MESH),
                  pltpu.make_async_remote_copy(src_ref=pool_ref, dst_ref=sib_pool, send_sem=a_send.at[1], recv_sem=a_recv.at[1],
                                               device_id=sib, device_id_type=MESH)]
        for cp in to_sib:
            cp.start()
        for cp in to_sib:
            cp.wait()
        mine_dm, theirs_dm = slab_ref[0:dm, :], sib_slab[0:dm, :]
        chip_slab[0:RED_ROWS, :] = slab_ref[dm:, :] + sib_slab[dm:, :]
        chip_slab[RED_ROWS:RED_ROWS + dm, :] = jnp.where(cc == 0, mine_dm, theirs_dm)
        chip_slab[RED_ROWS + dm:, :] = jnp.where(cc == 0, theirs_dm, mine_dm)
        chip_pool[...] = pool_ref[...] + sib_pool[...]

        sends = []
        for j, (dx, dy) in enumerate(CHIP_RELS):
            peer = (x ^ dx, y ^ dy, cc)
            sends.append(pltpu.make_async_remote_copy(src_ref=chip_slab, dst_ref=land_slab.at[chip], send_sem=b_send.at[j, 0],
                                                      recv_sem=b_recv.at[j, 0], device_id=peer, device_id_type=MESH))
            sends.append(pltpu.make_async_remote_copy(src_ref=chip_pool, dst_ref=land_pool.at[chip], send_sem=b_send.at[j, 1],
                                                      recv_sem=b_recv.at[j, 1], device_id=peer, device_id_type=MESH))
        for cp in sends:
            cp.start()
        land_slab[chip] = chip_slab[...]
        land_pool[chip] = chip_pool[...]
        for j, (dx, dy) in enumerate(CHIP_RELS):
            peer_chip = 2 * (x ^ dx) + (y ^ dy)
            pltpu.make_async_remote_copy(src_ref=chip_slab, dst_ref=land_slab.at[peer_chip], send_sem=b_send.at[j, 0],
                                         recv_sem=b_recv.at[j, 0], device_id=(x, y, cc), device_id_type=MESH).wait_recv()
            pltpu.make_async_remote_copy(src_ref=chip_pool, dst_ref=land_pool.at[peer_chip], send_sem=b_send.at[j, 1],
                                         recv_sem=b_recv.at[j, 1], device_id=(x, y, cc), device_id_type=MESH).wait_recv()
        red = land_slab[0, 0:RED_ROWS, :]
        ptot = land_pool[0]
        for ch in range(1, N_CHIP):
            red = red + land_slab[ch, 0:RED_ROWS, :]
            ptot = ptot + land_pool[ch]
        red_ref[...] = red
        ptot_ref[...] = ptot
        for ch in range(N_CHIP):
            dmod_ref[2 * dm * ch:2 * dm * (ch + 1), :] = land_slab[ch, RED_ROWS:, :]
        for cp in sends:
            cp.wait_send()

    dma = pltpu.SemaphoreType.DMA
    return _pcall(body, name=name, in_specs=[VMEM_SPEC, VMEM_SPEC], out_specs=[VMEM_SPEC] * 3,
                  out_shape=[_sds((RED_ROWS, w), F32), _sds((N_DEV * dm, w), F32), _sds((prow, pw), F32)],
                  scratch=[pltpu.VMEM((rows, w), F32), pltpu.VMEM((prow, pw), F32), pltpu.VMEM((crow, w), F32),
                           pltpu.VMEM((prow, pw), F32), pltpu.VMEM((N_CHIP, crow, w), F32), pltpu.VMEM((N_CHIP, prow, pw), F32),
                           dma((2,)), dma((2,)), dma((3, 2)), dma((3, 2))], comm=comm)(slab, pool)


def _adamw_math(w, g, m, v):
    mn = ADAM_B1 * m + (1.0 - ADAM_B1) * g
    vn = ADAM_B2 * v + (1.0 - ADAM_B2) * (g * g)
    bc1 = 1.0 / (1.0 - ADAM_B1 ** ADAM_STEP)
    bc2 = 1.0 / (1.0 - ADAM_B2 ** ADAM_STEP)
    return -ADAM_LR * ((mn * bc1) / (jnp.sqrt(vn * bc2) + ADAM_EPS) + ADAM_WD * w), mn, vn


def _small_update(red, dmod_all, pool_total, nb, params, *, name):
    names = list(SMALL)
    dm = dmod_all.shape[0] // N_DEV

    def grad_of(nm, red_ref, dmod_ref, ptot_ref):
        if nm == "b_ada":
            acc = None
            for d in range(N_DEV):
                for s in range(nb):
                    blk = dmod_ref[d * dm + 9 * s:d * dm + 9 * s + 9, :]
                    acc = blk if acc is None else acc + blk
            return jnp.concatenate([acc[k:k + 1, :] for k in range(9)], axis=1)
        if nm == "pool_grp":
            return ptot_ref[...]
        row, lo, n = {"norm_ffn1": (0, 0, D), "norm_mix": (1, 0, D), "norm_ffn2": (2, 0, D), "pool_scale": (3, 0, POOL_W),
                      "q_a_norm": (4, 0, QL), "kv_a_norm": (5, 0, KVL), "q_norm_nope": (6, 0, NOPE),
                      "q_norm_rope": (6, NOPE, ROPE), "k_norm_nope": (7, 0, NOPE), "k_norm_rope": (7, KR_LANE, ROPE)}[nm]
        return red_ref[row:row + 1, lo:lo + n]

    def body(*refs):
        red_ref, dmod_ref, ptot_ref = refs[:3]
        ins = refs[3:3 + 3 * len(names)]
        outs = refs[3 + 3 * len(names):]
        outs[4 * len(names)][...] = red_ref[RED_ROWS - 1:RED_ROWS, LOSS_LANE:]
        for i, nm in enumerate(names):
            g = grad_of(nm, red_ref, dmod_ref, ptot_ref)
            d, mn, vn = _adamw_math(ins[3 * i][...], g, ins[3 * i + 1][...], ins[3 * i + 2][...])
            outs[4 * i][...] = g
            outs[4 * i + 1][...] = d
            outs[4 * i + 2][...] = mn
            outs[4 * i + 3][...] = vn

    flat_in = [a for nm in names for a in params[nm]]
    out_shape = [_sds(params[nm][0].shape, F32) for nm in names for _ in range(4)] + [_sds((1, LANE), F32)]
    res = _pcall(body, name=name, in_specs=[VMEM_SPEC] * (3 + len(flat_in)), out_specs=[VMEM_SPEC] * len(out_shape),
                 out_shape=out_shape)(red, dmod_all, pool_total, *flat_in)
    return {nm: tuple(res[4 * i:4 * i + 4]) for i, nm in enumerate(names)}, res[-1]


def _adamw(w, g, m, v, *, name, comm=None):
    r, c = w.shape
    tr = _pick_rows(r, 256)
    tc = c if tr < r else _pick(c, 256)
    spec = pl.BlockSpec((tr, tc), lambda i, j: (i, j))
    bc1 = 1.0 / (1.0 - ADAM_B1 ** ADAM_STEP)
    bc2 = 1.0 / (1.0 - ADAM_B2 ** ADAM_STEP)

    def body(w_ref, g_ref, m_ref, v_ref, d_ref, mo_ref, vo_ref):
        gv = g_ref[...]
        mn = ADAM_B1 * m_ref[...] + (1.0 - ADAM_B1) * gv
        vn = ADAM_B2 * v_ref[...] + (1.0 - ADAM_B2) * (gv * gv)
        mo_ref[...] = mn
        vo_ref[...] = vn
        d_ref[...] = -ADAM_LR * ((mn * bc1) / (jnp.sqrt(vn * bc2) + ADAM_EPS) + ADAM_WD * w_ref[...])

    out = _sds((r, c), F32)
    return _pcall(body, name=name, out_shape=[out, out, out], grid=(r // tr, c // tc), in_specs=[spec] * 4, out_specs=[spec] * 3,
                  comm=comm)(w, g, m, v)


BIG = ("w_ffn1_in", "w_ffn1_out", "w_in", "w_pool_proj", "w_q_up", "w_kv_up", "w_mla_proj", "w_out", "w_ffn2_in", "w_ffn2_out")
ROW_SHARDED = ("w_ffn1_out", "w_out", "w_ffn2_out")
KERNEL_NAME = {"w_ffn1_in": "ffn1_in", "w_ffn1_out": "ffn1_out", "w_in": "w_in", "w_pool_proj": "pool_proj", "w_q_up": "q_up",
               "w_kv_up": "kv_up", "w_mla_proj": "mla_proj", "w_out": "w_out", "w_ffn2_in": "ffn2_in", "w_ffn2_out": "ffn2_out"}
WEIGHTS = ("w_ada", "b_ada", "norm_ffn1", "w_ffn1_in", "w_ffn1_out", "norm_mix", "w_in", "pool_grp", "pool_scale", "w_pool_proj",
           "q_a_norm", "w_q_up", "kv_a_norm", "w_kv_up", "q_norm_nope", "q_norm_rope", "k_norm_nope", "k_norm_rope", "w_mla_proj",
           "w_out", "norm_ffn2", "w_ffn2_in", "w_ffn2_out")
SMALL = ("b_ada", "norm_ffn1", "norm_mix", "pool_grp", "pool_scale", "q_a_norm", "kv_a_norm", "q_norm_nope", "q_norm_rope",
         "k_norm_nope", "k_norm_rope", "norm_ffn2")
SMALL_SEQ = tuple(n for n in SMALL if n != "pool_grp")
SLAB_W = 1024


def _assemble(name, stacked):
    if name in ROW_SHARDED:
        return stacked.reshape(stacked.shape[0] * stacked.shape[1], stacked.shape[2])
    return jnp.transpose(stacked, (1, 0, 2)).reshape(stacked.shape[1], stacked.shape[0] * stacked.shape[2])


def _split(name, full):
    if name in ROW_SHARDED:
        return full.reshape(N_CHIP, full.shape[0] // N_CHIP, full.shape[1])
    return jnp.transpose(full.reshape(full.shape[0], N_CHIP, full.shape[1] // N_CHIP), (1, 0, 2))


GU = ("w_ffn1_in", "w_ffn2_in")
NARROW = ("w_in", "w_q_up")


def _to_rows(v, width=SLAB_W):
    flat = v.reshape(-1)
    rows = -(-flat.shape[0] // width)
    return jnp.pad(flat, (0, rows * width - flat.shape[0])).reshape(rows, width)


def _pack_small(parts, names=SMALL):
    rows, spans, at = [], {}, 0
    for name in names:
        r = _to_rows(parts[name])
        spans[name] = (at, parts[name].size, parts[name].shape)
        rows.append(r)
        at += r.shape[0]
    pad = (-at) % SUBLANE
    if pad:
        rows.append(jnp.zeros((pad, SLAB_W), F32))
    return jnp.concatenate(rows, axis=0), spans


def _unpack_small(slab, spans):
    out = {}
    for name, (at, size, shape) in spans.items():
        nrow = -(-size // SLAB_W)
        out[name] = slab[at:at + nrow].reshape(-1)[:size].reshape(shape)
    return out


MIX_SMALL = ("w_out", "w_pool_proj", "w_mla_proj", "w_q_up", "w_kv_up")
RIDES = {
    "ada_prologue": (("gather_lo", ("w_ffn1_in",)),),
    "ffn1_up_a": (("gather_hi", ("w_ffn1_in",)),),
    "ffn1_up_b": (("gather", ("w_ffn1_out",)),),
    "ffn1_out": (("gather", ("w_in",)),),
    "mix_in": (("gather", MIX_SMALL),),
    "attn_fwd": (("gather", ("w_ffn2_in", "w_ffn2_out")),),
    "d_ffn2_h": (("swap", ("w_ffn2_out", "w_ffn2_in")),),
    "attn_bwd": (("exchange", ("w_ffn2_out", "w_ffn2_in")),),
    "d_mix_in": (("swap", MIX_SMALL),),
    "d_mix_h": (("exchange", MIX_SMALL), ("swap", ("w_in",))),
    "d_ffn1_out": (("exchange", ("w_in",)),),
    "ffn1_dact": (("swap", ("w_ffn1_out",)),),
    "d_ffn1_in_a": (("exchange", ("w_ffn1_out",)),),
    "d_ffn1_in_b": (("swap", ("w_ffn1_in@0",)),),
    "d_ffn1_h": (("exchange", ("w_ffn1_in@0",)), ("swap", ("w_ffn1_in@1",))),
    "bwd_norm1": (("exchange", ("w_ffn1_in@1",)),),
    "small_allreduce": (("join", tuple(n for n in BIG if n != "w_ffn1_in") + ("w_ffn1_in@0", "w_ffn1_in@1")),),
}


def _kname(n):
    base, _, part = n.partition("@")
    return KERNEL_NAME[base] + ("@" + part if part else "")


def _both(comms):
    if len(comms) == 1:
        return comms[0]
    ins, outs, sems, aliases, spans = [], [], [], {}, []
    for c in comms:
        spans.append((len(ins), len(c.ins), len(outs), len(c.out_shapes), len(sems), len(c.sems)))
        aliases.update({len(ins) + i: len(outs) + o for i, o in c.aliases.items()})
        ins, outs, sems = ins + c.ins, outs + c.out_shapes, sems + c.sems

    def each(which):
        def run(i_, o_, s_):
            for c, (ia, ni, oa, no, sa, ns) in zip(comms, spans):
                getattr(c, which)(i_[ia:ia + ni], o_[oa:oa + no], s_[sa:sa + ns])
        return run

    return _Comm(ins, outs, sems, each("start"), each("finish"), aliases)


class _ExchangePlan:
    def __init__(self, shards, cidx):
        self.shards, self.cidx = shards, cidx
        self.W, self.G, self.parts, self.pre, self.reduced, self.joined = {}, {}, {}, {}, {}, {}

    def grad(self, key, g):
        self.G[key] = g

    def rider(self, name):
        comms = []
        for kind, names in RIDES.get(name, ()):
            if kind.startswith("gather"):
                comms.append(_gather_comm([self.shards[n] for n in names], only_y={"gather_lo": 0, "gather_hi": 1}.get(kind)))
            elif kind == "swap":
                for n in names:
                    self.parts[n] = self._stacked(n)
                comms.append(_swap_comm([self.parts[n] for n in names]))
            elif kind == "exchange":
                comms.append(_exchange_comm([self.pre[n] for n in names]))
            else:
                comms.append(_join_comm([self.reduced[n] for n in names]))
        return _both(comms) if comms else None

    def landed(self, name, outs):
        at = 0
        for kind, names in RIDES[name]:
            for n, o in zip(names, outs[at:at + len(names)]):
                if kind.startswith("gather"):
                    self.W[KERNEL_NAME[n] + kind[len("gather"):]] = self._to_kernel(n, o)
                elif kind == "swap":
                    self.pre[n] = _add_half(self.parts[n], o, self.cidx, name="rs_add_" + _kname(n))
                elif kind == "exchange":
                    self.reduced[n] = _sum_chips(o, self.cidx, name="rs_sum_" + _kname(n))
                else:
                    self.joined[n] = o
            at += len(names)

    @staticmethod
    def _to_kernel(n, stacked):
        if n in GU:
            return stacked
        full = _assemble(n, stacked)
        return {"w_in": _w_in_to_kernel, "w_q_up": _q_up_to_kernel, "w_kv_up": _kv_up_to_kernel}.get(n, lambda w: w)(full)

    def _stacked(self, n):
        g = self.G[_kname(n)]
        if n.partition("@")[0] in GU:
            return g
        full = {"w_in": _w_in_from_kernel, "w_q_up": _q_up_from_kernel, "w_kv_up": _kv_up_from_kernel}.get(n, lambda w: w)(g)
        return _split(n, full)


def kernel(x, c, positions, w_ada, b_ada, norm_ffn1, w_ffn1_in, w_ffn1_out, norm_mix, w_in, pool_grp, pool_scale, w_pool_proj, q_a_norm, w_q_up, kv_a_norm, w_kv_up, q_norm_nope, q_norm_rope, k_norm_nope, k_norm_rope, w_mla_proj, w_out, norm_ffn2, w_ffn2_in, w_ffn2_out, loss_target, m_w_ada, m_b_ada, m_norm_ffn1, m_w_ffn1_in, m_w_ffn1_out, m_norm_mix, m_w_in, m_pool_grp, m_pool_scale, m_w_pool_proj, m_q_a_norm, m_w_q_up, m_kv_a_norm, m_w_kv_up, m_q_norm_nope, m_q_norm_rope, m_k_norm_nope, m_k_norm_rope, m_w_mla_proj, m_w_out, m_norm_ffn2, m_w_ffn2_in, m_w_ffn2_out, v_w_ada, v_b_ada, v_norm_ffn1, v_w_ffn1_in, v_w_ffn1_out, v_norm_mix, v_w_in, v_pool_grp, v_pool_scale, v_w_pool_proj, v_q_a_norm, v_w_q_up, v_kv_a_norm, v_w_kv_up, v_q_norm_nope, v_q_norm_rope, v_k_norm_nope, v_k_norm_rope, v_w_mla_proj, v_w_out, v_norm_ffn2, v_w_ffn2_in, v_w_ffn2_out):
    args = dict(locals())
    wts = {n: args[n][0] for n in WEIGHTS}
    mom = {n: args["m_" + n][0] for n in WEIGHTS}
    var = {n: args["v_" + n][0] for n in WEIGHTS}
    nb, seq, dm = x.shape
    tokens = nb * seq
    xi, yi, ci = _coords()
    chip = 2 * xi + yi
    dev = 4 * xi + 2 * yi + ci

    cidx = ci.astype(jnp.int32).reshape(1)
    plan = _ExchangePlan({n: wts[n].astype(MX) for n in BIG}, cidx)
    plan.W["pool_grp"] = wts["pool_grp"].astype(MX)

    ncol = w_ada.shape[2]
    b_cols = lax.dynamic_slice_in_dim(wts["b_ada"].reshape(1, -1), chip * ncol, ncol, axis=1)
    c_slots, mod_slots = _run(plan, _ada_prologue, c, wts["w_ada"], b_cols, name="ada_prologue")
    c_all = c_slots[:, :nb].reshape(N_DEV * nb, dm)
    mod3 = jnp.transpose(mod_slots[:, :nb], (1, 0, 2)).reshape(nb, 9, dm)
    P = {"norm_ffn1": wts["norm_ffn1"].reshape(1, dm), "norm_mix": wts["norm_mix"].reshape(1, dm),
         "norm_ffn2": wts["norm_ffn2"].reshape(1, dm), "pool_scale": wts["pool_scale"].reshape(1, POOL_W),
         "q_a_norm": wts["q_a_norm"].reshape(1, QL), "kv_a_norm": wts["kv_a_norm"].reshape(1, KVL),
         "q_gain": _gain_slab(wts["q_norm_nope"].reshape(1, NOPE), wts["q_norm_rope"].reshape(1, ROPE)),
         "k_gain": _gain_slab(wts["k_norm_nope"].reshape(1, NOPE), wts["k_norm_rope"].reshape(1, ROPE))}
    cos, sin = _rope_tables(positions.reshape(tokens))

    loss8, grad_x, stats, d_pool_grp = _layer_fwd_bwd(x.reshape(tokens, dm), loss_target.reshape(tokens, dm), mod3, cos, sin, plan, P)

    slab = _pack_stats(*stats, loss8, name="pack_stats")
    red, dmod_rows, pool_total = _run(plan, _small_allreduce, slab, d_pool_grp.reshape(POOL_G * LANE, LANE), name="small_allreduce")
    grads = {n: plan.joined[n] for n in BIG if n != "w_ffn1_in"}
    grads["w_ffn1_in"] = jnp.concatenate([plan.joined["w_ffn1_in@0"], plan.joined["w_ffn1_in@1"]], axis=0)
    dm_rows = dmod_rows.shape[0] // N_DEV
    dmod_all = dmod_rows.reshape(N_DEV, dm_rows, SLAB_W)[:, :9 * nb].reshape(N_DEV * nb, 9 * dm)
    dmod_cols = lax.dynamic_slice_in_dim(dmod_all, chip * ncol, ncol, axis=1)
    grads["w_ada"] = _ada_bwd(c_all, dmod_cols, name="ada_bwd")

    delta, new_m, new_v = {}, {}, {}
    as2d = lambda a: a.reshape(POOL_G * LANE, LANE) if a.ndim == 4 else a.reshape(1, -1)
    upd, loss_row = _small_update(red, dmod_rows, pool_total, nb,
                                  {n: tuple(as2d(args[p + n]) for p in ("", "m_", "v_")) for n in SMALL}, name="small_update")
    loss = loss_row[0, 0]
    for n in SMALL:
        grads[n], delta[n], new_m[n], new_v[n] = upd[n]
    for n in ("w_ada",) + BIG:
        if n in NARROW:
            res = _adamw(wts[n].T, grads[n].T, mom[n].T, var[n].T, name="adamw_" + n)
            delta[n], new_m[n], new_v[n] = (r.T for r in res)
        else:
            delta[n], new_m[n], new_v[n] = _adamw(wts[n], grads[n], mom[n], var[n], name="adamw_" + n)

    def lead(a, n):
        return a.reshape((1,) + wts[n].shape)

    return (loss, grad_x.reshape(nb, seq, dm), *[lead(grads[n], n) for n in WEIGHTS], *[lead(delta[n], n) for n in WEIGHTS],
            *[lead(new_m[n], n) for n in WEIGHTS], *[lead(new_v[n], n) for n in WEIGHTS])
```

```python
import functools
import math

import jax
import jax.numpy as jnp
from jax import lax
from jax.experimental import pallas as pl
from jax.experimental.pallas import tpu as pltpu

F32 = jnp.float32
MX = jnp.bfloat16

D = 1024
DFF = 2816
NH = 8
POOL_W = 512
POOL_G = 4
QL = 384
KVL = 256
ROPE = 32
NOPE = 64
LANE = 128
SUBLANE = 8
EPS = 1e-6
ATTN_SCALE = 1.0 / math.sqrt(96.0)
NEG = -1e30

Z_UP, Z_QL, Z_KV, Z_KR, Z_GP, Z_GM, Z_W = 0, 512, 896, 1152, 1280, 2304, 3328
KR_LANE = 64
LAT_W = 1280

ADAM_LR, ADAM_B1, ADAM_B2, ADAM_EPS, ADAM_WD, ADAM_STEP = 0.001, 0.9, 0.999, 1e-08, 0.01, 10

VMEM_LIMIT = 48 * 1024 * 1024
MESH = pl.DeviceIdType.MESH
N_DEV = 8
N_CHIP = 4


class _Comm:
    def __init__(self, ins, out_shapes, sems, start, finish, aliases=None):
        self.ins, self.out_shapes, self.sems = list(ins), list(out_shapes), list(sems)
        self.start, self.finish = start, finish
        self.aliases = aliases or {}


def _pcall(body, *, name, out_shape, grid=(), in_specs=None, out_specs=None, scratch=(), grid_spec=None, aliases=None,
           comm=None):
    params = pltpu.CompilerParams(vmem_limit_bytes=VMEM_LIMIT)
    kw = dict(name=name, compiler_params=params)
    if comm is None:
        if aliases:
            kw["input_output_aliases"] = aliases
        if grid_spec is not None:
            return pl.pallas_call(body, grid_spec=grid_spec, out_shape=out_shape, **kw)
        return pl.pallas_call(body, grid=grid, in_specs=in_specs, out_specs=out_specs, scratch_shapes=scratch,
                              out_shape=out_shape, **kw)
    single = not isinstance(out_shape, (list, tuple))
    outs = [out_shape] if single else list(out_shape)
    ospecs = [out_specs] if single else list(out_specs)
    n_in, n_out, n_ci, n_co, n_scr = len(in_specs), len(outs), len(comm.ins), len(comm.out_shapes), len(scratch)
    io = dict(aliases or {})
    io.update({n_in + i: n_out + o for i, o in comm.aliases.items()})

    def riding(*refs):
        ins, cins = refs[:n_in], refs[n_in:n_in + n_ci]
        at = n_in + n_ci
        os_, couts = refs[at:at + n_out], refs[at + n_out:at + n_out + n_co]
        at += n_out + n_co
        scr, csems = refs[at:at + n_scr], refs[at + n_scr:]
        if grid:
            first = functools.reduce(jnp.logical_and, [pl.program_id(d) == 0 for d in range(len(grid))])
            last = functools.reduce(jnp.logical_and, [pl.program_id(d) == grid[d] - 1 for d in range(len(grid))])
            pl.when(first)(lambda: comm.start(cins, couts, csems))
            body(*ins, *os_, *scr)
            pl.when(last)(lambda: comm.finish(cins, couts, csems))
        else:
            comm.start(cins, couts, csems)
            body(*ins, *os_, *scr)
            comm.finish(cins, couts, csems)

    call = pl.pallas_call(riding, grid=grid, in_specs=list(in_specs) + [HBM_SPEC] * n_ci, out_specs=ospecs + [HBM_SPEC] * n_co,
                          out_shape=outs + comm.out_shapes, scratch_shapes=list(scratch) + comm.sems,
                          input_output_aliases=io, **kw)

    def run(*args):
        res = call(*args, *comm.ins)
        main = list(res[:n_out])
        return (main[0] if single else main), list(res[n_out:])

    return run


def _pick(dim, target):
    best = None
    for t in range(LANE, min(dim, target) + 1, LANE):
        if dim % t == 0:
            best = t
    return dim if best is None else best


def _sds(shape, dtype):
    return jax.ShapeDtypeStruct(shape, dtype)


def _dw(a, g, *, name, tn=1024, comm=None):
    return _mm(a, g, mode="tn", out_dtype=F32, name=name, tm=256, tn=tn, tk=a.shape[0], n_outer=True, comm=comm)


def _mm(a, b, *, mode, out_dtype, name, tm=1024, tn=1024, tk=4096, n_outer=False, comm=None):
    if mode == "nn":
        (M, K), (K2, N) = a.shape, b.shape
    elif mode == "nt":
        (M, K), (N, K2) = a.shape, b.shape
    else:
        (K, M), (K2, N) = a.shape, b.shape
    assert K == K2, (name, a.shape, b.shape)
    tm, tn, tk = _pick(M, tm), _pick(N, tn), _pick(K, tk)
    nk = K // tk
    if n_outer:
        ij = lambda g0, g1: (g1, g0)
        grid = (N // tn, M // tm, nk)
    else:
        ij = lambda g0, g1: (g0, g1)
        grid = (M // tm, N // tn, nk)
    if mode == "tn":
        a_spec = pl.BlockSpec((tk, tm), lambda g0, g1, k: (k, ij(g0, g1)[0]))
    else:
        a_spec = pl.BlockSpec((tm, tk), lambda g0, g1, k: (ij(g0, g1)[0], k))
    if mode == "nt":
        b_spec = pl.BlockSpec((tn, tk), lambda g0, g1, k: (ij(g0, g1)[1], k))
    else:
        b_spec = pl.BlockSpec((tk, tn), lambda g0, g1, k: (k, ij(g0, g1)[1]))
    o_spec = pl.BlockSpec((tm, tn), lambda g0, g1, k: ij(g0, g1))
    dn = {"nn": (((1,), (0,)), ((), ())), "nt": (((1,), (1,)), ((), ())), "tn": (((0,), (0,)), ((), ()))}[mode]

    def dot(a_ref, b_ref):
        return lax.dot_general(a_ref[...].astype(MX), b_ref[...].astype(MX), dn, preferred_element_type=F32)

    def body_one(a_ref, b_ref, o_ref):
        o_ref[...] = dot(a_ref, b_ref).astype(o_ref.dtype)

    def body_acc(a_ref, b_ref, o_ref, acc_ref):
        k = pl.program_id(2)
        part = dot(a_ref, b_ref)

        @pl.when(k == 0)
        def _():
            acc_ref[...] = part

        @pl.when(k > 0)
        def _():
            acc_ref[...] += part

        @pl.when(k == nk - 1)
        def _():
            o_ref[...] = acc_ref[...].astype(o_ref.dtype)

    return _pcall(body_one if nk == 1 else body_acc, name=name, out_shape=_sds((M, N), out_dtype), grid=grid,
                  in_specs=[a_spec, b_spec], out_specs=o_spec, scratch=[] if nk == 1 else [pltpu.VMEM((tm, tn), F32)],
                  comm=comm)(a, b)


def _gu_shard(q):
    return (q % 2) * 2 + q // 2


def _ffn_up(h, w_st, *, name, tm=512, comm=None):
    T, dm = h.shape
    hw = w_st.shape[2]
    tm = _pick(T, tm)

    def body(h_ref, wg_ref, wu_ref, gu_ref, a_ref):
        hv = h_ref[...]
        g = jnp.dot(hv, wg_ref[0], preferred_element_type=F32)
        u = jnp.dot(hv, wu_ref[0], preferred_element_type=F32)
        gu_ref[:, :hw] = g.astype(gu_ref.dtype)
        gu_ref[:, hw:] = u.astype(gu_ref.dtype)
        a_ref[...] = (g * _sigmoid(g) * u).astype(a_ref.dtype)

    return _pcall(body, name=name, grid=(T // tm, 2),
                  in_specs=[pl.BlockSpec((tm, dm), lambda i, j: (i, 0)), pl.BlockSpec((1, dm, hw), lambda i, j: (j, 0, 0)),
                            pl.BlockSpec((1, dm, hw), lambda i, j: (2 + j, 0, 0))],
                  out_specs=[pl.BlockSpec((tm, 2 * hw), lambda i, j: (i, j)), pl.BlockSpec((tm, hw), lambda i, j: (i, j))],
                  out_shape=[_sds((T, 4 * hw), MX), _sds((T, 2 * hw), MX)], comm=comm)(h, w_st, w_st)


def _ffn_up_first(x, mod3, gain, w_lo, *, sub, name, tm=512, comm=None):
    T, dm = x.shape
    hw = w_lo.shape[2]
    tm = _pick(T, tm)
    tps = (T // mod3.shape[0]) // tm

    def body(x_ref, mod_ref, n_ref, wg_ref, wu_ref, h_ref, gu_ref, a_ref):
        xv = x_ref[...]
        xn = xv * _rsq(xv) * n_ref[...]
        hv = (xn * (1.0 + mod_ref[0, 3 * sub + 1:3 * sub + 2, :]) + mod_ref[0, 3 * sub:3 * sub + 1, :]).astype(h_ref.dtype)
        h_ref[...] = hv
        g = jnp.dot(hv, wg_ref[0], preferred_element_type=F32)
        u = jnp.dot(hv, wu_ref[0], preferred_element_type=F32)
        gu_ref[:, :hw] = g.astype(gu_ref.dtype)
        gu_ref[:, hw:] = u.astype(gu_ref.dtype)
        a_ref[...] = (g * _sigmoid(g) * u).astype(a_ref.dtype)

    return _pcall(body, name=name, grid=(T // tm,),
                  in_specs=[_row_spec(tm, dm), pl.BlockSpec((1, 9, dm), lambda i: (i // tps, 0, 0)),
                            pl.BlockSpec((1, dm), lambda i: (0, 0)), pl.BlockSpec((1, dm, hw), lambda i: (0, 0, 0)),
                            pl.BlockSpec((1, dm, hw), lambda i: (2, 0, 0))],
                  out_specs=[_row_spec(tm, dm), pl.BlockSpec((tm, 2 * hw), lambda i: (i, 0)), pl.BlockSpec((tm, hw), lambda i: (i, 0))],
                  out_shape=[_sds((T, dm), MX), _sds((T, 4 * hw), MX), _sds((T, 2 * hw), MX)],
                  comm=comm)(x, mod3, gain, w_lo, w_lo)


def _ffn_up_second(h, w_hi, gu, a, *, name, tm=512, comm=None):
    T, dm = h.shape
    hw = w_hi.shape[2]
    tm = _pick(T, tm)

    def body(h_ref, wg_ref, wu_ref, gu_in, a_in, gu_ref, a_ref):
        hv = h_ref[...]
        g = jnp.dot(hv, wg_ref[0], preferred_element_type=F32)
        u = jnp.dot(hv, wu_ref[0], preferred_element_type=F32)
        gu_ref[:, :hw] = g.astype(gu_ref.dtype)
        gu_ref[:, hw:] = u.astype(gu_ref.dtype)
        a_ref[...] = (g * _sigmoid(g) * u).astype(a_ref.dtype)

    return _pcall(body, name=name, grid=(T // tm,),
                  in_specs=[_row_spec(tm, dm), pl.BlockSpec((1, dm, hw), lambda i: (1, 0, 0)),
                            pl.BlockSpec((1, dm, hw), lambda i: (3, 0, 0)), HBM_SPEC, HBM_SPEC],
                  out_specs=[pl.BlockSpec((tm, 2 * hw), lambda i: (i, 1)), pl.BlockSpec((tm, hw), lambda i: (i, 1))],
                  out_shape=[_sds(gu.shape, gu.dtype), _sds(a.shape, a.dtype)], aliases={3: 0, 4: 1},
                  comm=comm)(h, w_hi, w_hi, gu, a)


def _ffn_dact(df, gu, w_out, *, name, tm=512, comm=None):
    T, dm = df.shape
    hw = gu.shape[1] // 4
    tm = _pick(T, tm)

    def body(df_ref, gu_ref, wo_ref, dgu_ref):
        da = lax.dot_general(df_ref[...], wo_ref[...], (((1,), (1,)), ((), ())), preferred_element_type=F32)
        g = gu_ref[:, :hw].astype(F32)
        u = gu_ref[:, hw:].astype(F32)
        s = _sigmoid(g)
        dgu_ref[:, :hw] = (da * u * (s * (1.0 + g * (1.0 - s)))).astype(dgu_ref.dtype)
        dgu_ref[:, hw:] = (da * (g * s)).astype(dgu_ref.dtype)

    return _pcall(body, name=name, grid=(T // tm, 2),
                  in_specs=[pl.BlockSpec((tm, dm), lambda i, j: (i, 0)), pl.BlockSpec((tm, 2 * hw), lambda i, j: (i, j)),
                            pl.BlockSpec((hw, dm), lambda i, j: (j, 0))],
                  out_specs=pl.BlockSpec((tm, 2 * hw), lambda i, j: (i, j)), out_shape=_sds(gu.shape, MX),
                  comm=comm)(df, gu, w_out)


def _ffn_dh(dgu, w_lo, w_hi, *, name, tm=1024, comm=None):
    T = dgu.shape[0]
    _, dm, hw = w_lo.shape
    tm = _pick(T, tm)

    def body(d_ref, lo_ref, hi_ref, o_ref, acc_ref):
        q = pl.program_id(1)

        def part(w_ref):
            return lax.dot_general(d_ref[...], w_ref[0], (((1,), (1,)), ((), ())), preferred_element_type=F32)

        @pl.when(q == 0)
        def _():
            acc_ref[...] = part(lo_ref)

        @pl.when(q == 1)
        def _():
            acc_ref[...] += part(lo_ref)

        @pl.when(q == 2)
        def _():
            acc_ref[...] += part(hi_ref)

        @pl.when(q == 3)
        def _():
            o_ref[...] = acc_ref[...] + part(hi_ref)

    return _pcall(body, name=name, grid=(T // tm, 4),
                  in_specs=[pl.BlockSpec((tm, hw), lambda i, q: (i, q)),
                            pl.BlockSpec((1, dm, hw), lambda i, q: (jnp.where(q < 2, _gu_shard(q), 2), 0, 0)),
                            pl.BlockSpec((1, dm, hw), lambda i, q: (jnp.where(q < 2, 1, _gu_shard(q)), 0, 0))],
                  out_specs=pl.BlockSpec((tm, dm), lambda i, q: (i, 0)), out_shape=_sds((T, dm), F32),
                  scratch=[pltpu.VMEM((tm, dm), F32)], comm=comm)(dgu, w_lo, w_hi)


def _ffn_dw_in(h, dgu, *, name, rows=None, tm=256, comm=None):
    T, dm = h.shape
    hw = dgu.shape[1] // 4
    first, count = rows if rows is not None else (0, dm)
    tm = _pick(count, tm)
    skip = first // tm

    def body(h_ref, d_ref, o_ref):
        o_ref[0] = lax.dot_general(h_ref[...], d_ref[...], (((0,), (0,)), ((), ())), preferred_element_type=F32)

    return _pcall(body, name=name, grid=(4, count // tm),
                  in_specs=[pl.BlockSpec((T, tm), lambda q, i: (0, skip + i)), pl.BlockSpec((T, hw), lambda q, i: (0, q))],
                  out_specs=pl.BlockSpec((1, tm, hw), lambda q, i: (_gu_shard(q), i, 0)),
                  out_shape=_sds((4, count, hw), F32), comm=comm)(h, dgu)


def _rsq(x):
    return lax.rsqrt(jnp.mean(x * x, axis=-1, keepdims=True) + EPS)


def _sigmoid(x):
    return 1.0 / (1.0 + jnp.exp(-x))


def _row_spec(tm, w):
    return pl.BlockSpec((tm, w), lambda i: (i, 0))


def _fwd_block(x_prev, f_prev, mod3, gain, *, sub, coef, name, tm=256):
    T, dm = x_prev.shape
    tps = (T // mod3.shape[0]) // tm
    has_f = f_prev is not None
    mod_spec = pl.BlockSpec((1, 9, dm), lambda i: (i // tps, 0, 0))
    vec_spec = pl.BlockSpec((1, dm), lambda i: (0, 0))

    def body(*refs):
        if has_f:
            x_ref, f_ref, mod_ref, n_ref, xo_ref, h_ref = refs
            x = x_ref[...] + coef * mod_ref[0, 3 * sub - 1:3 * sub, :] * f_ref[...]
            xo_ref[...] = x
        else:
            x_ref, mod_ref, n_ref, h_ref = refs
            x = x_ref[...]
        xn = x * _rsq(x) * n_ref[...]
        h = xn * (1.0 + mod_ref[0, 3 * sub + 1:3 * sub + 2, :]) + mod_ref[0, 3 * sub:3 * sub + 1, :]
        h_ref[...] = h.astype(h_ref.dtype)

    row = _row_spec(tm, dm)
    if has_f:
        return _pcall(body, name=name, grid=(T // tm,), in_specs=[row, row, mod_spec, vec_spec], out_specs=[row, row],
                      out_shape=[_sds((T, dm), F32), _sds((T, dm), MX)])(x_prev, f_prev, mod3, gain)
    return _pcall(body, name=name, grid=(T // tm,), in_specs=[row, mod_spec, vec_spec], out_specs=row,
                  out_shape=_sds((T, dm), MX))(x_prev, mod3, gain)


def _mm_resid_norm(a, w, x_prev, mod3, gain, *, sub, coef, name, tm=512, comm=None):
    T, k = a.shape
    dm = w.shape[1]
    tm = _pick(T, tm)
    tps = (T // mod3.shape[0]) // tm

    def body(a_ref, w_ref, x_ref, mod_ref, n_ref, f_ref, xo_ref, h_ref):
        f = jnp.dot(a_ref[...], w_ref[...], preferred_element_type=F32)
        f_ref[...] = f
        x = x_ref[...] + coef * mod_ref[0, 3 * sub - 1:3 * sub, :] * f
        xo_ref[...] = x
        xn = x * _rsq(x) * n_ref[...]
        h_ref[...] = (xn * (1.0 + mod_ref[0, 3 * sub + 1:3 * sub + 2, :]) + mod_ref[0, 3 * sub:3 * sub + 1, :]).astype(h_ref.dtype)

    row = _row_spec(tm, dm)
    return _pcall(body, name=name, grid=(T // tm,),
                  in_specs=[_row_spec(tm, k), pl.BlockSpec((k, dm), lambda i: (0, 0)), row,
                            pl.BlockSpec((1, 9, dm), lambda i: (i // tps, 0, 0)), pl.BlockSpec((1, dm), lambda i: (0, 0))],
                  out_specs=[row, row, row], out_shape=[_sds((T, dm), F32), _sds((T, dm), F32), _sds((T, dm), MX)],
                  comm=comm)(a, w, x_prev, mod3, gain)


def _mm_loss(a, w, x2, tgt, mod3, *, name, tm=512):
    T, k = a.shape
    dm = w.shape[1]
    tm = _pick(T, tm)
    tps = (T // mod3.shape[0]) // tm
    mod_spec = pl.BlockSpec((1, 9, dm), lambda i: (i // tps, 0, 0))
    row = _row_spec(tm, dm)
    stat_spec = pl.BlockSpec((1, SUBLANE, dm), lambda i: (i // tps, 0, 0))
    loss_spec = pl.BlockSpec((SUBLANE, LANE), lambda i: (0, 0))

    def body(a_ref, w_ref, x_ref, t_ref, mod_ref, dy_ref, df_ref, st_ref, loss_ref):
        i = pl.program_id(0)
        g = mod_ref[0, 8:9, :]
        f = jnp.dot(a_ref[...], w_ref[...], preferred_element_type=F32)
        err = x_ref[...] + 0.5 * g * f - t_ref[...]
        dy = err * (1.0 / dm)
        dy_ref[...] = dy
        df_ref[...] = (0.5 * g * dy).astype(df_ref.dtype)
        dgate = jnp.sum(0.5 * dy * f, axis=0, keepdims=True)
        part = 0.5 * jnp.sum(jnp.sum(err * err, axis=0, keepdims=True), axis=1, keepdims=True) * (1.0 / dm)

        @pl.when(i % tps == 0)
        def _():
            st_ref[...] = jnp.zeros_like(st_ref)

        @pl.when(i == 0)
        def _():
            loss_ref[...] = jnp.zeros_like(loss_ref)

        st_ref[0, 0:1, :] += dgate
        loss_ref[...] += jnp.broadcast_to(part, loss_ref.shape)

    return _pcall(body, name=name, grid=(T // tm,),
                  in_specs=[_row_spec(tm, k), pl.BlockSpec((k, dm), lambda i: (0, 0)), row, row, mod_spec],
                  out_specs=[row, row, stat_spec, loss_spec],
                  out_shape=[_sds((T, dm), F32), _sds((T, dm), MX), _sds((mod3.shape[0], SUBLANE, dm), F32),
                             _sds((SUBLANE, LANE), F32)])(a, w, x2, tgt, mod3)


def _norm_bwd_tail(dhv, x_ref, dxi_ref, f_ref, mod_ref, n_ref, dx_ref, df_ref, st_ref, *, first, sub, coef):
    x = x_ref[...]
    r = _rsq(x)
    xhat = x * r
    n = n_ref[...]
    d_shift = jnp.sum(dhv, axis=0, keepdims=True)
    d_scale = jnp.sum(dhv * (xhat * n), axis=0, keepdims=True)
    dxn = dhv * (1.0 + mod_ref[0, 3 * sub + 1:3 * sub + 2, :])
    d_gain = jnp.sum(dxn * xhat, axis=0, keepdims=True)
    dxhat = dxn * n
    dx = dxi_ref[...] + r * (dxhat - xhat * jnp.mean(dxhat * xhat, axis=-1, keepdims=True))
    dx_ref[...] = dx

    @pl.when(first)
    def _():
        st_ref[...] = jnp.zeros_like(st_ref)

    st_ref[0, 0:1, :] += d_shift
    st_ref[0, 1:2, :] += d_scale
    st_ref[0, 2:3, :] += d_gain
    if f_ref is not None:
        st_ref[0, 3:4, :] += jnp.sum(coef * dx * f_ref[...], axis=0, keepdims=True)
        df_ref[...] = (coef * mod_ref[0, 3 * sub - 1:3 * sub, :] * dx).astype(df_ref.dtype)


def _ffn_dh_norm(dgu, w_st, x_cur, dx_in, f_prev, mod3, gain, *, sub, coef, name, tm=512, comm=None):
    T = dgu.shape[0]
    _, dm, hw = w_st.shape
    tm = _pick(T, tm)
    nb = mod3.shape[0]
    tps = (T // nb) // tm

    def body(d_ref, w_ref, x_ref, dxi_ref, f_ref, mod_ref, n_ref, dx_ref, df_ref, st_ref, acc_ref):
        i = pl.program_id(0)
        q = pl.program_id(1)
        part = lax.dot_general(d_ref[...], w_ref[0], (((1,), (1,)), ((), ())), preferred_element_type=F32)

        @pl.when(q == 0)
        def _():
            acc_ref[...] = part

        @pl.when(q > 0)
        def _():
            acc_ref[...] += part

        @pl.when(q == 3)
        def _():
            _norm_bwd_tail(acc_ref[...], x_ref, dxi_ref, f_ref, mod_ref, n_ref, dx_ref, df_ref, st_ref,
                           first=i % tps == 0, sub=sub, coef=coef)

    row = pl.BlockSpec((tm, dm), lambda i, q: (i, 0))
    return _pcall(body, name=name, grid=(T // tm, 4),
                  in_specs=[pl.BlockSpec((tm, hw), lambda i, q: (i, q)), pl.BlockSpec((1, dm, hw), lambda i, q: (_gu_shard(q), 0, 0)),
                            row, row, row, pl.BlockSpec((1, 9, dm), lambda i, q: (i // tps, 0, 0)),
                            pl.BlockSpec((1, dm), lambda i, q: (0, 0))],
                  out_specs=[row, row, pl.BlockSpec((1, SUBLANE, dm), lambda i, q: (i // tps, 0, 0))],
                  out_shape=[_sds((T, dm), F32), _sds((T, dm), MX), _sds((nb, SUBLANE, dm), F32)],
                  scratch=[pltpu.VMEM((tm, dm), F32)], comm=comm)(dgu, w_st, x_cur, dx_in, f_prev, mod3, gain)


def _mm_nt_norm(a, b, x_cur, dx_in, f_prev, mod3, gain, *, sub, coef, name, tm=512, comm=None):
    T, k = a.shape
    dm = b.shape[0]
    tm = _pick(T, tm)
    nb = mod3.shape[0]
    tps = (T // nb) // tm

    def body(a_ref, b_ref, x_ref, dxi_ref, f_ref, mod_ref, n_ref, dx_ref, df_ref, st_ref):
        dh = lax.dot_general(a_ref[...], b_ref[...], (((1,), (1,)), ((), ())), preferred_element_type=F32)
        _norm_bwd_tail(dh, x_ref, dxi_ref, f_ref, mod_ref, n_ref, dx_ref, df_ref, st_ref,
                       first=pl.program_id(0) % tps == 0, sub=sub, coef=coef)

    row = _row_spec(tm, dm)
    return _pcall(body, name=name, grid=(T // tm,),
                  in_specs=[_row_spec(tm, k), pl.BlockSpec((dm, k), lambda i: (0, 0)), row, row, row,
                            pl.BlockSpec((1, 9, dm), lambda i: (i // tps, 0, 0)), pl.BlockSpec((1, dm), lambda i: (0, 0))],
                  out_specs=[row, row, pl.BlockSpec((1, SUBLANE, dm), lambda i: (i // tps, 0, 0))],
                  out_shape=[_sds((T, dm), F32), _sds((T, dm), MX), _sds((nb, SUBLANE, dm), F32)],
                  comm=comm)(a, b, x_cur, dx_in, f_prev, mod3, gain)


def _bwd_block(x_cur, dh, dx_in, f_prev, mod3, gain, *, sub, coef, name, tm=256, comm=None):
    T, dm = x_cur.shape
    nb = mod3.shape[0]
    tps = (T // nb) // tm
    has_f = f_prev is not None
    mod_spec = pl.BlockSpec((1, 9, dm), lambda i: (i // tps, 0, 0))
    vec_spec = pl.BlockSpec((1, dm), lambda i: (0, 0))
    stat_spec = pl.BlockSpec((1, SUBLANE, dm), lambda i: (i // tps, 0, 0))
    row = _row_spec(tm, dm)

    def body(*refs):
        if has_f:
            x_ref, dh_ref, dxi_ref, f_ref, mod_ref, n_ref, dx_ref, df_ref, st_ref = refs
        else:
            x_ref, dh_ref, dxi_ref, mod_ref, n_ref, dx_ref, st_ref = refs
            f_ref = df_ref = None
        _norm_bwd_tail(dh_ref[...], x_ref, dxi_ref, f_ref, mod_ref, n_ref, dx_ref, df_ref, st_ref,
                       first=pl.program_id(0) % tps == 0, sub=sub, coef=coef)

    st_shape = _sds((nb, SUBLANE, dm), F32)
    if has_f:
        return _pcall(body, name=name, grid=(T // tm,), in_specs=[row, row, row, row, mod_spec, vec_spec],
                      out_specs=[row, row, stat_spec],
                      out_shape=[_sds((T, dm), F32), _sds((T, dm), MX), st_shape], comm=comm)(x_cur, dh, dx_in, f_prev, mod3, gain)
    return _pcall(body, name=name, grid=(T // tm,), in_specs=[row, row, row, mod_spec, vec_spec],
                  out_specs=[row, stat_spec], out_shape=[_sds((T, dm), F32), st_shape], comm=comm)(x_cur, dh, dx_in, mod3, gain)


def _merge_fwd(z, br_pool, br_mla, *, name, tm=256):
    T = z.shape[0]

    def body(z_ref, bp_ref, bm_ref, o_ref):
        gp = z_ref[:, Z_GP:Z_GP + D]
        gm = z_ref[:, Z_GM:Z_GM + D]
        o_ref[...] = (_sigmoid(gp) * bp_ref[...] + _sigmoid(gm) * bm_ref[...]).astype(o_ref.dtype)

    return _pcall(body, name=name, grid=(T // tm,), in_specs=[_row_spec(tm, Z_W), _row_spec(tm, D), _row_spec(tm, D)],
                  out_specs=_row_spec(tm, D), out_shape=_sds((T, D), MX))(z, br_pool, br_mla)


def _merge_bwd(z, br_pool, br_mla, dmerged, *, name, tm=256):
    T = z.shape[0]

    def body(z_ref, bp_ref, bm_ref, dm_ref, dbp_ref, dbm_ref, dg_ref):
        dm = dm_ref[...]
        sp = _sigmoid(z_ref[:, Z_GP:Z_GP + D])
        sm = _sigmoid(z_ref[:, Z_GM:Z_GM + D])
        dbp_ref[...] = (dm * sp).astype(dbp_ref.dtype)
        dbm_ref[...] = (dm * sm).astype(dbm_ref.dtype)
        dg_ref[:, :D] = (dm * bp_ref[...] * sp * (1.0 - sp)).astype(dg_ref.dtype)
        dg_ref[:, D:] = (dm * bm_ref[...] * sm * (1.0 - sm)).astype(dg_ref.dtype)

    return _pcall(body, name=name, grid=(T // tm,),
                  in_specs=[_row_spec(tm, Z_W), _row_spec(tm, D), _row_spec(tm, D), _row_spec(tm, D)],
                  out_specs=[_row_spec(tm, D), _row_spec(tm, D), _row_spec(tm, 2 * D)],
                  out_shape=[_sds((T, D), MX), _sds((T, D), MX), _sds((T, 2 * D), MX)])(z, br_pool, br_mla, dmerged)


def _shift_down(x, k, row):
    return jnp.where(row >= k, pltpu.roll(x, k, 0), 0.0)


def _shift_up(x, k, row, n):
    return jnp.where(row < n - k, pltpu.roll(x, n - k, 0), 0.0)


def _pool_fwd(z, pool_grp, pool_scale, *, nb, name):
    T = z.shape[0]
    S = T // nb
    blk = pl.BlockSpec((S, LANE), lambda b, g: (b, g))

    def body(u_ref, w_ref, s_ref, pooled_ref, mixed_ref, scaled_ref):
        g = pl.program_id(1)
        u = u_ref[...]
        row = lax.broadcasted_iota(jnp.int32, u.shape, 0)
        s2 = u + _shift_down(u, 1, row)
        s4 = s2 + _shift_down(s2, 2, row)
        s8 = s4 + _shift_down(s4, 4, row)
        s16 = s8 + _shift_down(s8, 8, row)
        win = jnp.where(g == 0, s2, jnp.where(g == 1, s4, jnp.where(g == 2, s8, s16)))
        width = lax.shift_left(jnp.int32(2), g)
        cnt = jnp.minimum(row + 1, width).astype(F32)
        pooled = (win / cnt - u).astype(MX)
        pooled_ref[...] = pooled
        mixed = jnp.dot(pooled, w_ref[0], preferred_element_type=F32)
        mixed_ref[...] = mixed
        scaled_ref[...] = (mixed * s_ref[...]).astype(scaled_ref.dtype)

    return _pcall(body, name=name, grid=(nb, POOL_G),
                  in_specs=[blk, pl.BlockSpec((1, LANE, LANE), lambda b, g: (g, 0, 0)),
                            pl.BlockSpec((1, LANE), lambda b, g: (0, g))],
                  out_specs=[blk, blk, blk],
                  out_shape=[_sds((T, POOL_W), MX), _sds((T, POOL_W), F32), _sds((T, POOL_W), MX)])(z, pool_grp, pool_scale)


def _pool_bwd(dscaled, mixed, pooled, pool_grp, pool_scale, *, nb, name):
    T = dscaled.shape[0]
    S = T // nb
    blk = pl.BlockSpec((S, LANE), lambda g, b: (b, g))

    def body(ds_ref, mixed_ref, pooled_ref, w_ref, s_ref, du_ref, dw_ref, dsc_ref):
        g = pl.program_id(0)
        b = pl.program_id(1)
        ds = ds_ref[...]
        dsc_ref[0] = jnp.sum(ds * mixed_ref[...], axis=0, keepdims=True)
        dmixed = (ds * s_ref[...]).astype(MX)
        dw = lax.dot_general(pooled_ref[...], dmixed, (((0,), (0,)), ((), ())), preferred_element_type=F32)

        @pl.when(b == 0)
        def _():
            dw_ref[0] = dw

        @pl.when(b > 0)
        def _():
            dw_ref[0] += dw
        dpooled = lax.dot_general(dmixed, w_ref[0], (((1,), (1,)), ((), ())), preferred_element_type=F32)
        row = lax.broadcasted_iota(jnp.int32, dpooled.shape, 0)
        width = lax.shift_left(jnp.int32(2), g)
        q = dpooled / jnp.minimum(row + 1, width).astype(F32)
        r2 = q + _shift_up(q, 1, row, S)
        r4 = r2 + _shift_up(r2, 2, row, S)
        r8 = r4 + _shift_up(r4, 4, row, S)
        r16 = r8 + _shift_up(r8, 8, row, S)
        win = jnp.where(g == 0, r2, jnp.where(g == 1, r4, jnp.where(g == 2, r8, r16)))
        du_ref[...] = (win - dpooled).astype(du_ref.dtype)

    return _pcall(body, name=name, grid=(POOL_G, nb),
                  in_specs=[blk, blk, blk, pl.BlockSpec((1, LANE, LANE), lambda g, b: (g, 0, 0)),
                            pl.BlockSpec((1, LANE), lambda g, b: (0, g))],
                  out_specs=[blk, pl.BlockSpec((1, LANE, LANE), lambda g, b: (g, 0, 0)),
                             pl.BlockSpec((1, 1, LANE), lambda g, b: (b, 0, g))],
                  out_shape=[_sds((T, POOL_W), MX), _sds((POOL_G, LANE, LANE), F32), _sds((nb, 1, POOL_W), F32)],
                  )(dscaled, mixed, pooled, pool_grp, pool_scale)


def _lat_fwd(z, q_gain, kv_gain, *, name, tm=256):
    T = z.shape[0]

    def body(z_ref, qg_ref, kg_ref, qn_ref, kvn_ref):
        ql = z_ref[:, Z_QL:Z_QL + QL]
        kv = z_ref[:, Z_KV:Z_KV + KVL]
        qn_ref[...] = (ql * _rsq(ql) * qg_ref[...]).astype(qn_ref.dtype)
        kvn_ref[...] = (kv * _rsq(kv) * kg_ref[...]).astype(kvn_ref.dtype)

    return _pcall(body, name=name, grid=(T // tm,),
                  in_specs=[_row_spec(tm, LAT_W), pl.BlockSpec((1, QL), lambda i: (0, 0)), pl.BlockSpec((1, KVL), lambda i: (0, 0))],
                  out_specs=[_row_spec(tm, QL), _row_spec(tm, KVL)],
                  out_shape=[_sds((T, QL), MX), _sds((T, KVL), MX)])(z, q_gain, kv_gain)


def _lat_bwd(z, dqn, dkvn, q_gain, kv_gain, *, nb, name, tm=256):
    T = z.shape[0]
    tps = (T // nb) // tm

    def norm_bwd(x, dy, gain):
        r = _rsq(x)
        xhat = x * r
        dgain = jnp.sum(dy * xhat, axis=0, keepdims=True)
        dxhat = dy * gain
        return r * (dxhat - xhat * jnp.mean(dxhat * xhat, axis=-1, keepdims=True)), dgain

    def body(z_ref, dq_ref, dkv_ref, qg_ref, kg_ref, dql_ref, dkvl_ref, sq_ref, sk_ref):
        i = pl.program_id(0)
        dql, dqg = norm_bwd(z_ref[:, Z_QL:Z_QL + QL], dq_ref[...], qg_ref[...])
        dkvl, dkg = norm_bwd(z_ref[:, Z_KV:Z_KV + KVL], dkv_ref[...], kg_ref[...])
        dql_ref[...] = dql.astype(dql_ref.dtype)
        dkvl_ref[...] = dkvl.astype(dkvl_ref.dtype)

        @pl.when(i % tps == 0)
        def _():
            sq_ref[...] = jnp.zeros_like(sq_ref)
            sk_ref[...] = jnp.zeros_like(sk_ref)

        sq_ref[0, 0:1, :] += dqg
        sk_ref[0, 0:1, :] += dkg

    return _pcall(body, name=name, grid=(T // tm,),
                  in_specs=[_row_spec(tm, LAT_W), _row_spec(tm, QL), _row_spec(tm, KVL),
                            pl.BlockSpec((1, QL), lambda i: (0, 0)), pl.BlockSpec((1, KVL), lambda i: (0, 0))],
                  out_specs=[_row_spec(tm, QL), _row_spec(tm, KVL),
                             pl.BlockSpec((1, SUBLANE, QL), lambda i: (i // tps, 0, 0)),
                             pl.BlockSpec((1, SUBLANE, KVL), lambda i: (i // tps, 0, 0))],
                  out_shape=[_sds((T, QL), MX), _sds((T, KVL), MX), _sds((nb, SUBLANE, QL), F32), _sds((nb, SUBLANE, KVL), F32)],
                  )(z, dqn, dkvn, q_gain, kv_gain)


def _lane_masks(shape):
    lane = lax.broadcasted_iota(jnp.int32, shape, len(shape) - 1)
    m_n = lane < NOPE
    m_r = jnp.logical_and(lane >= KR_LANE, lane < KR_LANE + ROPE)
    first_half = lane < KR_LANE + ROPE // 2
    return m_n, m_r, first_half


def _rot(y, first_half):
    return jnp.where(first_half, -pltpu.roll(y, LANE - ROPE // 2, 1), pltpu.roll(y, ROPE // 2, 1))


def _rot_t(v, first_half, m_r):
    return jnp.where(m_r, jnp.where(first_half, pltpu.roll(v, LANE - ROPE // 2, 1), -pltpu.roll(v, ROPE // 2, 1)), 0.0)


def _prep_fwd(qp, kvp, z, cos, sin, q_gain, k_gain, *, name, tm=256):
    T = qp.shape[0]
    slab = pl.BlockSpec((tm, LANE), lambda i: (i, 0))
    kr_spec = pl.BlockSpec((tm, LANE), lambda i: (i, Z_KR // LANE))
    vec = pl.BlockSpec((1, LANE), lambda i: (0, 0))

    def body(qp_ref, kvp_ref, kr_ref, cos_ref, sin_ref, qg_ref, kg_ref, q_ref, k_ref, v_ref):
        m_n, m_r, first_half = _lane_masks((tm, LANE))
        c = cos_ref[...]
        s = sin_ref[...]
        qg = qg_ref[...]
        kg = kg_ref[...]
        xr = kr_ref[...]
        rr = lax.rsqrt(jnp.sum(xr * xr, axis=-1, keepdims=True) * (1.0 / ROPE) + EPS)
        yr = xr * rr * kg
        kr = jnp.where(m_r, yr * c + _rot(yr, first_half) * s, 0.0)
        for h in range(NH):
            x = qp_ref[:, h * LANE:(h + 1) * LANE]
            x2 = x * x
            rn = lax.rsqrt(jnp.sum(jnp.where(m_n, x2, 0.0), axis=-1, keepdims=True) * (1.0 / NOPE) + EPS)
            rq = lax.rsqrt(jnp.sum(jnp.where(m_r, x2, 0.0), axis=-1, keepdims=True) * (1.0 / ROPE) + EPS)
            y = x * jnp.where(m_n, rn, jnp.where(m_r, rq, 0.0)) * qg
            q_ref[:, h * LANE:(h + 1) * LANE] = (y * c + _rot(y, first_half) * s).astype(q_ref.dtype)
            xk = kvp_ref[:, h * LANE:(h + 1) * LANE]
            rk = lax.rsqrt(jnp.sum(jnp.where(m_n, xk * xk, 0.0), axis=-1, keepdims=True) * (1.0 / NOPE) + EPS)
            k_ref[:, h * LANE:(h + 1) * LANE] = (jnp.where(m_n, xk * rk * kg, 0.0) + kr).astype(k_ref.dtype)
        v_ref[...] = kvp_ref[:, NH * LANE:].astype(v_ref.dtype)

    return _pcall(body, name=name, grid=(T // tm,),
                  in_specs=[_row_spec(tm, NH * LANE), _row_spec(tm, NH * LANE + NH * NOPE), kr_spec, slab, slab, vec, vec],
                  out_specs=[_row_spec(tm, NH * LANE), _row_spec(tm, NH * LANE), _row_spec(tm, NH * NOPE)],
                  out_shape=[_sds((T, NH * LANE), MX), _sds((T, NH * LANE), MX), _sds((T, NH * NOPE), MX)],
                  )(qp, kvp, z, cos, sin, q_gain, k_gain)


def _prep_bwd(dq, dk, dv, qp, kvp, z, cos, sin, q_gain, k_gain, *, nb, name, tm=256):
    T = qp.shape[0]
    tps = (T // nb) // tm
    slab = pl.BlockSpec((tm, LANE), lambda i: (i, 0))
    kr_spec = pl.BlockSpec((tm, LANE), lambda i: (i, Z_KR // LANE))
    vec = pl.BlockSpec((1, LANE), lambda i: (0, 0))

    def body(dq_ref, dk_ref, dv_ref, qp_ref, kvp_ref, kr_ref, cos_ref, sin_ref, qg_ref, kg_ref,
             dqp_ref, dkvp_ref, dkr_ref, st_ref):
        i = pl.program_id(0)
        m_n, m_r, first_half = _lane_masks((tm, LANE))
        c = cos_ref[...]
        s = sin_ref[...]
        qg = qg_ref[...]
        kg = kg_ref[...]
        dqg = jnp.zeros((1, LANE), F32)
        dkg = jnp.zeros((1, LANE), F32)
        dkr_sum = jnp.zeros((tm, LANE), F32)
        for h in range(NH):
            x = qp_ref[:, h * LANE:(h + 1) * LANE]
            x2 = x * x
            rn = lax.rsqrt(jnp.sum(jnp.where(m_n, x2, 0.0), axis=-1, keepdims=True) * (1.0 / NOPE) + EPS)
            rq = lax.rsqrt(jnp.sum(jnp.where(m_r, x2, 0.0), axis=-1, keepdims=True) * (1.0 / ROPE) + EPS)
            rfac = jnp.where(m_n, rn, jnp.where(m_r, rq, 0.0))
            xhat = x * rfac
            do = dq_ref[:, h * LANE:(h + 1) * LANE]
            dy = do * c + _rot_t(do * s, first_half, m_r)
            dqg = dqg + jnp.sum(dy * xhat, axis=0, keepdims=True)
            dxhat = dy * qg
            t = dxhat * xhat
            mean_n = jnp.sum(jnp.where(m_n, t, 0.0), axis=-1, keepdims=True) * (1.0 / NOPE)
            mean_r = jnp.sum(jnp.where(m_r, t, 0.0), axis=-1, keepdims=True) * (1.0 / ROPE)
            dqp_ref[:, h * LANE:(h + 1) * LANE] = (
                rfac * (dxhat - xhat * jnp.where(m_n, mean_n, jnp.where(m_r, mean_r, 0.0)))).astype(dqp_ref.dtype)

            xk = kvp_ref[:, h * LANE:(h + 1) * LANE]
            rk = lax.rsqrt(jnp.sum(jnp.where(m_n, xk * xk, 0.0), axis=-1, keepdims=True) * (1.0 / NOPE) + EPS)
            khat = jnp.where(m_n, xk * rk, 0.0)
            dko = dk_ref[:, h * LANE:(h + 1) * LANE]
            dkn = jnp.where(m_n, dko, 0.0)
            dkg = dkg + jnp.sum(dkn * khat, axis=0, keepdims=True)
            dkhat = dkn * kg
            mean_k = jnp.sum(dkhat * khat, axis=-1, keepdims=True) * (1.0 / NOPE)
            dkvp_ref[:, h * LANE:(h + 1) * LANE] = jnp.where(m_n, rk * (dkhat - khat * mean_k), 0.0).astype(dkvp_ref.dtype)
            dkr_sum = dkr_sum + jnp.where(m_r, dko, 0.0)
        dkvp_ref[:, NH * LANE:] = dv_ref[...].astype(dkvp_ref.dtype)

        xr = kr_ref[...]
        rr = lax.rsqrt(jnp.sum(xr * xr, axis=-1, keepdims=True) * (1.0 / ROPE) + EPS)
        rhat = xr * rr
        dyr = dkr_sum * c + _rot_t(dkr_sum * s, first_half, m_r)
        dkg = dkg + jnp.sum(dyr * rhat, axis=0, keepdims=True)
        drhat = dyr * kg
        mean_kr = jnp.sum(drhat * rhat, axis=-1, keepdims=True) * (1.0 / ROPE)
        dkr_ref[...] = jnp.where(m_r, rr * (drhat - rhat * mean_kr), 0.0).astype(dkr_ref.dtype)

        @pl.when(i % tps == 0)
        def _():
            st_ref[...] = jnp.zeros_like(st_ref)

        st_ref[0, 0:1, :] += dqg
        st_ref[0, 1:2, :] += dkg

    return _pcall(body, name=name, grid=(T // tm,),
                  in_specs=[_row_spec(tm, NH * LANE), _row_spec(tm, NH * LANE), _row_spec(tm, NH * NOPE),
                            _row_spec(tm, NH * LANE), _row_spec(tm, NH * LANE + NH * NOPE), kr_spec, slab, slab, vec, vec],
                  out_specs=[_row_spec(tm, NH * LANE), _row_spec(tm, NH * LANE + NH * NOPE), slab,
                             pl.BlockSpec((1, SUBLANE, LANE), lambda i: (i // tps, 0, 0))],
                  out_shape=[_sds((T, NH * LANE), MX), _sds((T, NH * LANE + NH * NOPE), MX), _sds((T, LANE), MX),
                             _sds((nb, SUBLANE, LANE), F32)],
                  )(dq, dk, dv, qp, kvp, z, cos, sin, q_gain, k_gain)


def _lower_triangle(t):
    return lax.broadcasted_iota(jnp.int32, (t, t), 1) <= lax.broadcasted_iota(jnp.int32, (t, t), 0)


def _attn_fwd(q, k, v, *, nb, name, tq=512, comm=None):
    T = q.shape[0]
    S = T // nb
    tq = _pick(S, tq)
    nq = S // tq
    tk = tq
    npair = NH // 2

    def body(q_ref, k_ref, v_ref, o_ref, lse_ref):
        qi = pl.program_id(2)
        lane = lax.broadcasted_iota(jnp.int32, (tq, LANE), 1)
        qs = [q_ref[:, hh * LANE:(hh + 1) * LANE] for hh in range(2)]

        def block(j, carry, diagonal):
            k0 = pl.multiple_of(j * tk, tk)
            vb = v_ref[pl.ds(k0, tk), :]
            new = []
            for hh in range(2):
                m, l, acc = carry[hh]
                kb = k_ref[pl.ds(k0, tk), hh * LANE:(hh + 1) * LANE]
                s = lax.dot_general(qs[hh], kb, (((1,), (1,)), ((), ())), preferred_element_type=F32) * ATTN_SCALE
                if diagonal:
                    s = jnp.where(_lower_triangle(tq), s, NEG)
                m_new = jnp.maximum(m, jnp.max(s, axis=-1, keepdims=True))
                p = jnp.exp(s - m_new)
                alpha = jnp.exp(m - m_new)
                l = alpha * l + jnp.sum(p, axis=-1, keepdims=True)
                acc = alpha * acc + jnp.dot(p.astype(MX), vb, preferred_element_type=F32)
                new.append((m_new, l, acc))
            return tuple(new)

        init = tuple((jnp.full((tq, 1), NEG, F32), jnp.zeros((tq, 1), F32), jnp.zeros((tq, LANE), F32)) for _ in range(2))
        carry = lax.fori_loop(0, qi, lambda j, c: block(j, c, False), init)
        (m0, l0, acc0), (m1, l1, acc1) = block(qi, carry, True)
        o_ref[...] = jnp.where(lane < NOPE, acc0 / l0, acc1 / l1).astype(o_ref.dtype)
        lse_ref[...] = jnp.where(lane < NOPE, m0 + jnp.log(l0), m1 + jnp.log(l1))

    return _pcall(body, name=name, grid=(nb, npair, nq),
                  in_specs=[pl.BlockSpec((tq, 2 * LANE), lambda b, p, i: (b * nq + i, p)),
                            pl.BlockSpec((S, 2 * LANE), lambda b, p, i: (b, p)),
                            pl.BlockSpec((S, LANE), lambda b, p, i: (b, p))],
                  out_specs=[pl.BlockSpec((tq, LANE), lambda b, p, i: (b * nq + i, p)),
                             pl.BlockSpec((tq, LANE), lambda b, p, i: (b * nq + i, p))],
                  out_shape=[_sds((T, NH * NOPE), MX), _sds((T, NH * NOPE), F32)], comm=comm)(q, k, v)


def _attn_bwd(q, k, v, o, lse, do, *, nb, name, tq=512, comm=None):
    T = q.shape[0]
    S = T // nb
    tq = _pick(S, tq)
    nq = S // tq
    tk = tq
    npair = NH // 2

    def body(q_ref, k_ref, v_ref, o_ref, lse_ref, do_ref, dq_ref, dk_ref, dv_ref, delta_ref):
        lane = lax.broadcasted_iota(jnp.int32, (tq, LANE), 1)
        first = lane < NOPE
        dq_ref[...] = jnp.zeros_like(dq_ref)

        def delta_step(qi, _):
            q0 = pl.multiple_of(qi * tq, tq)
            prod = do_ref[pl.ds(q0, tq), :] * o_ref[pl.ds(q0, tq), :].astype(F32)
            d0 = jnp.sum(jnp.where(first, prod, 0.0), axis=-1, keepdims=True)
            d1 = jnp.sum(jnp.where(first, 0.0, prod), axis=-1, keepdims=True)
            delta_ref[pl.ds(q0, tq), :] = jnp.where(first, d0, d1)
            return 0

        lax.fori_loop(0, nq, delta_step, 0)

        def kv_step(kj, _):
            k0 = pl.multiple_of(kj * tk, tk)
            kbs = [k_ref[pl.ds(k0, tk), hh * LANE:(hh + 1) * LANE] for hh in range(2)]
            vb = v_ref[pl.ds(k0, tk), :]

            def q_block(qi, carry, diagonal):
                dk0, dk1, dv = carry
                dks = [dk0, dk1]
                q0 = pl.multiple_of(qi * tq, tq)
                dov = do_ref[pl.ds(q0, tq), :]
                lse_v = lse_ref[pl.ds(q0, tq), :]
                delta_v = delta_ref[pl.ds(q0, tq), :]
                for hh in range(2):
                    qh = q_ref[pl.ds(q0, tq), hh * LANE:(hh + 1) * LANE]
                    dob = jnp.where(first if hh == 0 else jnp.logical_not(first), dov, 0.0).astype(MX)
                    s = lax.dot_general(qh, kbs[hh], (((1,), (1,)), ((), ())), preferred_element_type=F32) * ATTN_SCALE
                    p = jnp.exp(s - lse_v[:, hh * NOPE:hh * NOPE + 1])
                    if diagonal:
                        p = jnp.where(_lower_triangle(tq), p, 0.0)
                    dp = lax.dot_general(dob, vb, (((1,), (1,)), ((), ())), preferred_element_type=F32)
                    ds = (p * (dp - delta_v[:, hh * NOPE:hh * NOPE + 1]) * ATTN_SCALE).astype(MX)
                    dks[hh] = dks[hh] + lax.dot_general(ds, qh, (((0,), (0,)), ((), ())), preferred_element_type=F32)
                    dv = dv + lax.dot_general(p.astype(MX), dob, (((0,), (0,)), ((), ())), preferred_element_type=F32)
                    dq_ref[pl.ds(q0, tq), hh * LANE:(hh + 1) * LANE] += jnp.dot(ds, kbs[hh], preferred_element_type=F32)
                return dks[0], dks[1], dv

            zero = jnp.zeros((tk, LANE), F32)
            carry = q_block(kj, (zero, zero, zero), True)
            dk0, dk1, dv = lax.fori_loop(kj + 1, nq, lambda qi, c: q_block(qi, c, False), carry)
            dk_ref[pl.ds(k0, tk), 0:LANE] = dk0
            dk_ref[pl.ds(k0, tk), LANE:2 * LANE] = dk1
            dv_ref[pl.ds(k0, tk), :] = dv
            return 0

        lax.fori_loop(0, nq, kv_step, 0)

    pair256 = pl.BlockSpec((S, 2 * LANE), lambda b, p: (b, p))
    pair128 = pl.BlockSpec((S, LANE), lambda b, p: (b, p))
    return _pcall(body, name=name, grid=(nb, npair),
                  in_specs=[pair256, pair256, pair128, pair128, pair128, pair128],
                  out_specs=[pair256, pair256, pair128],
                  out_shape=[_sds((T, NH * LANE), F32), _sds((T, NH * LANE), F32), _sds((T, NH * NOPE), F32)],
                  scratch=[pltpu.VMEM((S, LANE), F32)], comm=comm)(q, k, v, o, lse, do)


class _NoExchange:
    def __init__(self, weights):
        self.W, self.G = weights, {}

    def rider(self, name):
        return None

    def landed(self, name, outs):
        pass

    def grad(self, key, g):
        self.G[key] = g


def _run(plan, fn, *args, name, **kw):
    rider = plan.rider(name)
    if rider is None:
        return fn(*args, name=name, **kw)
    outs, landed = fn(*args, name=name, comm=rider, **kw)
    plan.landed(name, landed)
    return outs


def _layer_fwd_bwd(x, tgt, mod3, cos, sin, plan, P):
    nb = mod3.shape[0]
    W = plan.W
    h1, gu1, a1 = _run(plan, _ffn_up_first, x, mod3, P["norm_ffn1"], W["ffn1_in_lo"], sub=0, name="ffn1_up_a")
    gu1, a1 = _run(plan, _ffn_up_second, h1, W["ffn1_in_hi"], gu1, a1, name="ffn1_up_b")
    f1, x1, h2 = _run(plan, _mm_resid_norm, a1, W["ffn1_out"], x, mod3, P["norm_mix"], sub=1, coef=0.5, name="ffn1_out")
    z = _run(plan, _mm, h2, W["w_in"], mode="nn", out_dtype=F32, name="mix_in", tn=1664)
    pooled, mixed, scaled = _pool_fwd(z, W["pool_grp"], P["pool_scale"], nb=nb, name="pool_fwd")
    br_pool = _mm(scaled, W["pool_proj"], mode="nn", out_dtype=F32, name="pool_proj")
    qn, kvn = _lat_fwd(z, P["q_a_norm"], P["kv_a_norm"], name="lat_fwd")
    qp = _mm(qn, W["q_up"], mode="nn", out_dtype=F32, name="q_up")
    kvp = _mm(kvn, W["kv_up"], mode="nn", out_dtype=F32, name="kv_up")
    q, k, v = _prep_fwd(qp, kvp, z, cos, sin, P["q_gain"], P["k_gain"], name="prep_fwd")
    attn, lse = _run(plan, _attn_fwd, q, k, v, nb=nb, name="attn_fwd")
    br_mla = _mm(attn, W["mla_proj"], mode="nn", out_dtype=F32, name="mla_proj")
    merged = _merge_fwd(z, br_pool, br_mla, name="merge_fwd")
    mo, x2, h3 = _mm_resid_norm(merged, W["w_out"], x1, mod3, P["norm_ffn2"], sub=2, coef=1.0, name="mix_out")
    gu2, a2 = _ffn_up(h3, W["ffn2_in"], name="ffn2_up")
    dy, df2, st_fin, loss = _mm_loss(a2, W["ffn2_out"], x2, tgt, mod3, name="ffn2_out_loss")

    plan.grad("ffn2_out", _dw(a2, df2, name="d_ffn2_out"))
    dgu2 = _ffn_dact(df2, gu2, W["ffn2_out"], name="ffn2_dact")
    plan.grad("ffn2_in", _ffn_dw_in(h3, dgu2, name="d_ffn2_in", tm=512))
    dx2, dmo, st3 = _run(plan, _ffn_dh_norm, dgu2, W["ffn2_in"], x2, dy, mo, mod3, P["norm_ffn2"], sub=2, coef=1.0,
                         name="d_ffn2_h")
    plan.grad("w_out", _dw(merged, dmo, name="d_mix_out"))
    dmerged = _mm(dmo, W["w_out"], mode="nt", out_dtype=F32, name="d_merged")
    dbr_pool, dbr_mla, dgates = _merge_bwd(z, br_pool, br_mla, dmerged, name="merge_bwd")
    plan.grad("pool_proj", _dw(scaled, dbr_pool, name="d_pool_proj"))
    dscaled = _mm(dbr_pool, W["pool_proj"], mode="nt", out_dtype=F32, name="d_pool_scaled")
    du_pool, d_pool_grp, d_pool_scale = _pool_bwd(dscaled, mixed, pooled, W["pool_grp"], P["pool_scale"], nb=nb, name="pool_bwd")
    plan.grad("mla_proj", _dw(attn, dbr_mla, name="d_mla_proj"))
    dattn = _mm(dbr_mla, W["mla_proj"], mode="nt", out_dtype=F32, name="d_attn")
    dq, dk, dv = _run(plan, _attn_bwd, q, k, v, attn, lse, dattn, nb=nb, name="attn_bwd")
    dqp, dkvp, dkr, st_prep = _prep_bwd(dq, dk, dv, qp, kvp, z, cos, sin, P["q_gain"], P["k_gain"], nb=nb, name="prep_bwd")
    plan.grad("q_up", _dw(qn, dqp, name="d_q_up"))
    dqn = _mm(dqp, W["q_up"], mode="nt", out_dtype=F32, name="d_qn")
    plan.grad("kv_up", _dw(kvn, dkvp, name="d_kv_up"))
    dkvn = _mm(dkvp, W["kv_up"], mode="nt", out_dtype=F32, name="d_kvn")
    dql, dkvl, st_q, st_kv = _lat_bwd(z, dqn, dkvn, P["q_a_norm"], P["kv_a_norm"], nb=nb, name="lat_bwd")
    dz = jnp.concatenate([du_pool, dql, dkvl, dkr, dgates], axis=1)
    plan.grad("w_in", _run(plan, _dw, h2, dz, name="d_mix_in", tn=1664))
    dx1, df1, st2 = _run(plan, _mm_nt_norm, dz, W["w_in"], x1, dx2, f1, mod3, P["norm_mix"], sub=1, coef=0.5, name="d_mix_h")
    plan.grad("ffn1_out", _run(plan, _dw, a1, df1, name="d_ffn1_out"))
    dgu1 = _run(plan, _ffn_dact, df1, gu1, W["ffn1_out"], name="ffn1_dact")
    half = x.shape[1] // 2
    plan.grad("ffn1_in@0", _run(plan, _ffn_dw_in, h1, dgu1, rows=(0, half), name="d_ffn1_in_a"))
    plan.grad("ffn1_in@1", _run(plan, _ffn_dw_in, h1, dgu1, rows=(half, half), name="d_ffn1_in_b"))
    dh1 = _run(plan, _ffn_dh, dgu1, W["ffn1_in_lo"], W["ffn1_in_hi"], name="d_ffn1_h")
    grad_x, st1 = _run(plan, _bwd_block, x, dh1, dx1, None, mod3, P["norm_ffn1"], sub=0, coef=0.0, name="bwd_norm1")

    return loss, grad_x, (st1, st2, st3, st_fin, st_q, st_kv, st_prep, d_pool_scale), d_pool_grp


def _w_in_to_kernel(w):
    k = w.shape[0]
    zeros = lambda n: jnp.zeros((k, n), w.dtype)
    return jnp.concatenate([w[:, 0:1152], zeros(KR_LANE), w[:, 1152:1184], zeros(LANE - KR_LANE - ROPE), w[:, 1184:]], axis=1)


def _w_in_from_kernel(g):
    return jnp.concatenate([g[:, 0:1152], g[:, Z_KR + KR_LANE:Z_KR + KR_LANE + ROPE], g[:, Z_GP:]], axis=1)


def _q_up_to_kernel(w):
    k = w.shape[0]
    return jnp.pad(w.reshape(k, NH, NOPE + ROPE), ((0, 0), (0, 0), (0, LANE - NOPE - ROPE))).reshape(k, NH * LANE)


def _q_up_from_kernel(g):
    k = g.shape[0]
    return g.reshape(k, NH, LANE)[:, :, :NOPE + ROPE].reshape(k, NH * (NOPE + ROPE))


def _kv_up_to_kernel(w):
    k = w.shape[0]
    w3 = w.reshape(k, NH, 2 * NOPE)
    kpart = jnp.pad(w3[:, :, :NOPE], ((0, 0), (0, 0), (0, LANE - NOPE))).reshape(k, NH * LANE)
    return jnp.concatenate([kpart, w3[:, :, NOPE:].reshape(k, NH * NOPE)], axis=1)


def _kv_up_from_kernel(g):
    k = g.shape[0]
    kpart = g[:, :NH * LANE].reshape(k, NH, LANE)[:, :, :NOPE]
    vpart = g[:, NH * LANE:].reshape(k, NH, NOPE)
    return jnp.concatenate([kpart, vpart], axis=2).reshape(k, NH * 2 * NOPE)


def _gain_slab(nope, rope):
    return jnp.concatenate([nope, rope, jnp.zeros((1, LANE - NOPE - ROPE), nope.dtype)], axis=1)


def _rope_tables(positions):
    inv_freq = 10000.0 ** (-jnp.arange(0, ROPE, 2, dtype=F32) / ROPE)
    ang = positions.astype(F32)[:, None] * inv_freq
    ang = jnp.concatenate([ang, ang], axis=-1)
    t = positions.shape[0]
    cos = jnp.concatenate([jnp.ones((t, KR_LANE), F32), jnp.cos(ang), jnp.ones((t, LANE - KR_LANE - ROPE), F32)], axis=1)
    sin = jnp.concatenate([jnp.zeros((t, KR_LANE), F32), jnp.sin(ang), jnp.zeros((t, LANE - KR_LANE - ROPE), F32)], axis=1)
    return cos, sin


def _coords():
    return lax.axis_index("x"), lax.axis_index("y"), lax.axis_index("c")


HBM_SPEC = pl.BlockSpec(memory_space=pl.ANY)
VMEM_SPEC = pl.BlockSpec(memory_space=pltpu.VMEM)


def _gather8(v, *, name, comm=None):
    r, c = v.shape

    def body(v_ref, out_ref, send_sems, recv_sems, local_sem):
        x, y, cc = _coords()
        me = 4 * x + 2 * y + cc
        mine = pltpu.make_async_copy(v_ref, out_ref.at[me], local_sem)
        mine.start()
        copies = []
        for kk in range(1, N_DEV):
            peer = (x ^ (kk >> 2), y ^ ((kk >> 1) & 1), cc ^ (kk & 1))
            cp = pltpu.make_async_remote_copy(src_ref=v_ref, dst_ref=out_ref.at[me], send_sem=send_sems.at[kk - 1],
                                              recv_sem=recv_sems.at[kk - 1], device_id=peer, device_id_type=MESH)
            cp.start()
            copies.append(cp)
        for kk in range(1, N_DEV):
            peer_slot = me ^ kk
            pltpu.make_async_remote_copy(src_ref=v_ref, dst_ref=out_ref.at[peer_slot], send_sem=send_sems.at[kk - 1],
                                         recv_sem=recv_sems.at[kk - 1], device_id=(x, y, cc), device_id_type=MESH).wait_recv()
        for cp in copies:
            cp.wait_send()
        mine.wait()

    return _pcall(body, name=name, out_shape=_sds((N_DEV, r, c), v.dtype), in_specs=[VMEM_SPEC], out_specs=VMEM_SPEC,
                  scratch=[pltpu.SemaphoreType.DMA((N_DEV - 1,)), pltpu.SemaphoreType.DMA((N_DEV - 1,)), pltpu.SemaphoreType.DMA],
                  comm=comm)(v)


CHIP_RELS = ((1, 0), (0, 1), (1, 1))


def _comm_only(comm, *, name):
    _, outs = _pcall(lambda: None, name=name, out_shape=[], in_specs=[], out_specs=[], comm=comm)()
    return outs


def _gather_comm(shards, only_y=None):
    n = len(shards)

    def sends(y):
        return True if only_y is None else y == only_y

    def if_(cond, fn):
        if cond is True:
            fn()
        else:
            pl.when(cond)(fn)

    def ici(ins, outs, sems, a, j, x, y, cc):
        half = ins[a].shape[0] // 2
        mine = pl.ds(cc * half, half)
        dx, dy = CHIP_RELS[j]
        return pltpu.make_async_remote_copy(src_ref=ins[a].at[mine], dst_ref=outs[a].at[2 * x + y, mine],
                                            send_sem=sems[0].at[a, j], recv_sem=sems[1].at[a, j],
                                            device_id=(x ^ dx, y ^ dy, cc), device_id_type=MESH)

    def d2d(ins, outs, sems, a, j, x, y, cc, half_of):
        half = ins[a].shape[0] // 2
        dx, dy = CHIP_RELS[j]
        landed = outs[a].at[2 * (x ^ dx) + (y ^ dy), pl.ds(half_of * half, half)]
        return pltpu.make_async_remote_copy(src_ref=landed, dst_ref=landed, send_sem=sems[2].at[a, j], recv_sem=sems[3].at[a, j],
                                            device_id=(x, y, 1 - cc), device_id_type=MESH)

    def own(ins, outs, sems, a, x, y):
        return pltpu.make_async_copy(ins[a], outs[a].at[2 * x + y], sems[4].at[a])

    def start(ins, outs, sems):
        x, y, cc = _coords()

        def go():
            for a in range(n):
                own(ins, outs, sems, a, x, y).start()
                for j in range(3):
                    ici(ins, outs, sems, a, j, x, y, cc).start()

        if_(sends(y), go)

    def finish(ins, outs, sems):
        x, y, cc = _coords()
        for j, (dx, dy) in enumerate(CHIP_RELS):
            def land_and_pass(j=j):
                for a in range(n):
                    ici(ins, outs, sems, a, j, x, y, cc).wait_recv()
                    d2d(ins, outs, sems, a, j, x, y, cc, cc).start()

            if_(sends(y ^ dy), land_and_pass)
        for j, (dx, dy) in enumerate(CHIP_RELS):
            def from_sibling(j=j):
                for a in range(n):
                    d2d(ins, outs, sems, a, j, x, y, cc, 1 - cc).wait_recv()

            if_(sends(y ^ dy), from_sibling)
        for j, (dx, dy) in enumerate(CHIP_RELS):
            def passed(j=j):
                for a in range(n):
                    d2d(ins, outs, sems, a, j, x, y, cc, cc).wait_send()

            if_(sends(y ^ dy), passed)

        def sent():
            for a in range(n):
                for j in range(3):
                    ici(ins, outs, sems, a, j, x, y, cc).wait_send()
                own(ins, outs, sems, a, x, y).wait()

        if_(sends(y), sent)

    dma = pltpu.SemaphoreType.DMA
    return _Comm(shards, [_sds((N_CHIP,) + s.shape, s.dtype) for s in shards],
                 [dma((n, 3)), dma((n, 3)), dma((n, 3)), dma((n, 3)), dma((n,))], start, finish)


def _swap_comm(parts):
    n = len(parts)

    def copy(ins, outs, sems, a):
        x, y, cc = _coords()
        half = ins[a].shape[1] // 2
        return pltpu.make_async_remote_copy(src_ref=ins[a].at[:, pl.ds((1 - cc) * half, half)], dst_ref=outs[a],
                                            send_sem=sems[0].at[a], recv_sem=sems[1].at[a], device_id=(x, y, 1 - cc),
                                            device_id_type=MESH)

    def start(ins, outs, sems):
        for a in range(n):
            copy(ins, outs, sems, a).start()

    def finish(ins, outs, sems):
        for a in range(n):
            copy(ins, outs, sems, a).wait()

    dma = pltpu.SemaphoreType.DMA
    return _Comm(parts, [_sds((p.shape[0], p.shape[1] // 2, p.shape[2]), p.dtype) for p in parts], [dma((n,)), dma((n,))],
                 start, finish)


def _add_half(full, other, cidx, *, name):
    nch, r, c = full.shape
    half = r // 2
    tr = _pick_rows(half)
    nbk = half // tr
    grid_spec = pltpu.PrefetchScalarGridSpec(
        num_scalar_prefetch=1, grid=(nch, nbk),
        in_specs=[pl.BlockSpec((1, tr, c), lambda j, i, cref: (j, cref[0] * nbk + i, 0)),
                  pl.BlockSpec((1, tr, c), lambda j, i, cref: (j, i, 0))],
        out_specs=pl.BlockSpec((1, tr, c), lambda j, i, cref: (j, i, 0)))

    def body(cref, a_ref, b_ref, o_ref):
        o_ref[...] = (a_ref[...] + b_ref[...]).astype(o_ref.dtype)

    return _pcall(body, name=name, out_shape=_sds((nch, half, c), MX), grid_spec=grid_spec)(cidx, full, other)


def _pick_rows(rows, target=512):
    best = None
    for t in range(16, min(rows, target) + 1, 16):
        if rows % t == 0:
            best = t
    return rows if best is None else best


def _exchange_comm(parts):
    n = len(parts)

    def send(ins, outs, sems, a, j, x, y, cc):
        dx, dy = CHIP_RELS[j]
        return pltpu.make_async_remote_copy(src_ref=ins[a].at[2 * (x ^ dx) + (y ^ dy)], dst_ref=outs[a].at[2 * x + y],
                                            send_sem=sems[0].at[a, j], recv_sem=sems[1].at[a, j],
                                            device_id=(x ^ dx, y ^ dy, cc), device_id_type=MESH)

    def landing(ins, outs, sems, a, j, x, y, cc):
        dx, dy = CHIP_RELS[j]
        peer_chip = 2 * (x ^ dx) + (y ^ dy)
        return pltpu.make_async_remote_copy(src_ref=ins[a].at[peer_chip], dst_ref=outs[a].at[peer_chip], send_sem=sems[0].at[a, j],
                                            recv_sem=sems[1].at[a, j], device_id=(x, y, cc), device_id_type=MESH)

    def own(ins, outs, sems, a, x, y):
        return pltpu.make_async_copy(ins[a].at[2 * x + y], outs[a].at[2 * x + y], sems[2].at[a])

    def start(ins, outs, sems):
        x, y, cc = _coords()
        for a in range(n):
            own(ins, outs, sems, a, x, y).start()
            for j in range(3):
                send(ins, outs, sems, a, j, x, y, cc).start()

    def finish(ins, outs, sems):
        x, y, cc = _coords()
        for a in range(n):
            for j in range(3):
                landing(ins, outs, sems, a, j, x, y, cc).wait_recv()
        for a in range(n):
            for j in range(3):
                send(ins, outs, sems, a, j, x, y, cc).wait_send()
            own(ins, outs, sems, a, x, y).wait()

    dma = pltpu.SemaphoreType.DMA
    return _Comm(parts, [_sds(p.shape, p.dtype) for p in parts], [dma((n, 3)), dma((n, 3)), dma((n,))], start, finish)


def _sum_chips(q, cidx, *, name):
    nch, h, c = q.shape
    tr = _pick_rows(h)
    nbk = h // tr
    grid_spec = pltpu.PrefetchScalarGridSpec(
        num_scalar_prefetch=1, grid=(nbk,),
        in_specs=[pl.BlockSpec((nch, tr, c), lambda i, cref: (0, i, 0))],
        out_specs=pl.BlockSpec((tr, c), lambda i, cref: (cref[0] * nbk + i, 0)))

    def body(cref, q_ref, o_ref):
        acc = q_ref[0].astype(F32) + q_ref[1].astype(F32)
        acc = acc + q_ref[2].astype(F32)
        o_ref[...] = acc + q_ref[3].astype(F32)

    return _pcall(body, name=name, out_shape=_sds((2 * h, c), F32), grid_spec=grid_spec)(cidx, q)


def _join_comm(fulls):
    n = len(fulls)

    def half_copy(outs, sems, a, which):
        x, y, cc = _coords()
        h = outs[a].shape[0] // 2
        rows = outs[a].at[pl.ds((cc if which == 0 else 1 - cc) * h, h)]
        return pltpu.make_async_remote_copy(src_ref=rows, dst_ref=rows, send_sem=sems[0].at[a], recv_sem=sems[1].at[a],
                                            device_id=(x, y, 1 - cc), device_id_type=MESH)

    def start(ins, outs, sems):
        for a in range(n):
            half_copy(outs, sems, a, 0).start()

    def finish(ins, outs, sems):
        for a in range(n):
            half_copy(outs, sems, a, 1).wait_recv()
        for a in range(n):
            half_copy(outs, sems, a, 0).wait_send()

    dma = pltpu.SemaphoreType.DMA
    return _Comm(fulls, [_sds(p.shape, p.dtype) for p in fulls], [dma((n,)), dma((n,))], start, finish,
                 aliases={a: a for a in range(n)})


def _ada_prologue(c, w, b, *, name, comm=None):
    nb, dm = c.shape
    n = w.shape[1]

    def body(c_ref, w_ref, b_ref, call_ref, land_ref, cpad, mod_s, g_send, g_recv, m_send, m_recv):
        x, y, cc = _coords()
        me = 4 * x + 2 * y + cc
        chip = 2 * x + y
        cpad[...] = jnp.zeros_like(cpad)
        cpad[0:nb, :] = c_ref[...]
        copies = []
        for kk in range(1, N_DEV):
            peer = (x ^ (kk >> 2), y ^ ((kk >> 1) & 1), cc ^ (kk & 1))
            cp = pltpu.make_async_remote_copy(src_ref=cpad, dst_ref=call_ref.at[me], send_sem=g_send.at[kk - 1],
                                              recv_sem=g_recv.at[kk - 1], device_id=peer, device_id_type=MESH)
            cp.start()
            copies.append(cp)
        call_ref[me] = cpad[...]
        for kk in range(1, N_DEV):
            pltpu.make_async_remote_copy(src_ref=cpad, dst_ref=call_ref.at[me ^ kk], send_sem=g_send.at[kk - 1],
                                         recv_sem=g_recv.at[kk - 1], device_id=(x, y, cc), device_id_type=MESH).wait_recv()
        cv = call_ref[...].reshape(N_DEV * SUBLANE, dm)
        act = (cv * _sigmoid(cv)).astype(MX)
        mod = jnp.dot(act, w_ref[...].astype(MX), preferred_element_type=F32) + b_ref[...]
        mod_s[...] = mod.reshape(N_DEV, SUBLANE, n)
        for j, (dx, dy) in enumerate(CHIP_RELS):
            cp = pltpu.make_async_remote_copy(src_ref=mod_s.at[4 * (x ^ dx) + 2 * (y ^ dy) + cc], dst_ref=land_ref.at[chip],
                                              send_sem=m_send.at[j], recv_sem=m_recv.at[j],
                                              device_id=(x ^ dx, y ^ dy, cc), device_id_type=MESH)
            cp.start()
            copies.append(cp)
        land_ref[chip] = mod_s[me]
        for j, (dx, dy) in enumerate(CHIP_RELS):
            pltpu.make_async_remote_copy(src_ref=mod_s.at[me], dst_ref=land_ref.at[2 * (x ^ dx) + (y ^ dy)], send_sem=m_send.at[j],
                                         recv_sem=m_recv.at[j], device_id=(x, y, cc), device_id_type=MESH).wait_recv()
        for cp in copies:
            cp.wait_send()

    dma = pltpu.SemaphoreType.DMA
    return _pcall(body, name=name, in_specs=[VMEM_SPEC] * 3, out_specs=[VMEM_SPEC] * 2,
                  out_shape=[_sds((N_DEV, SUBLANE, dm), F32), _sds((N_CHIP, SUBLANE, n), F32)],
                  scratch=[pltpu.VMEM((SUBLANE, dm), F32), pltpu.VMEM((N_DEV, SUBLANE, n), F32),
                           dma((N_DEV - 1,)), dma((N_DEV - 1,)), dma((3,)), dma((3,))], comm=comm)(c, w, b)


def _ada_bwd(c_all, dmod_cols, *, name):
    m, kdim = c_all.shape
    n = dmod_cols.shape[1]
    tn = _pick(n, 1152)

    def body(c_ref, d_ref, o_ref):
        cv = c_ref[...]
        act = (cv * _sigmoid(cv)).astype(MX)
        o_ref[...] = lax.dot_general(act, d_ref[...].astype(MX), (((0,), (0,)), ((), ())), preferred_element_type=F32)

    return _pcall(body, name=name, out_shape=_sds((kdim, n), F32), grid=(n // tn,),
                  in_specs=[pl.BlockSpec((m, kdim), lambda j: (0, 0)), pl.BlockSpec((m, tn), lambda j: (0, j))],
                  out_specs=pl.BlockSpec((kdim, tn), lambda j: (0, j)))(c_all, dmod_cols)


SLAB_W = 1024
RED_ROWS = 8


LOSS_LANE = SLAB_W - LANE


def _pack_stats(st1, st2, st3, st_fin, st_q, st_kv, st_prep, d_pool_scale, loss8, *, name):
    nb = st1.shape[0]
    dm_rows = -(-9 * nb // SUBLANE) * SUBLANE

    def body(s1, s2, s3, sf, sq, skv, sp, sps, loss_ref, o_ref):
        o_ref[...] = jnp.zeros_like(o_ref)
        for s in range(nb):
            rows = [s1[s, 0:1, :], s1[s, 1:2, :], s2[s, 3:4, :], s2[s, 0:1, :], s2[s, 1:2, :], s3[s, 3:4, :], s3[s, 0:1, :],
                    s3[s, 1:2, :], sf[s, 0:1, :]]
            for k, row in enumerate(rows):
                o_ref[9 * s + k:9 * s + k + 1, :] = row

        def over_seq(ref, r):
            acc = ref[0, r:r + 1, :]
            for s in range(1, nb):
                acc = acc + ref[s, r:r + 1, :]
            return acc

        o_ref[dm_rows + 0:dm_rows + 1, :] = over_seq(s1, 2)
        o_ref[dm_rows + 1:dm_rows + 2, :] = over_seq(s2, 2)
        o_ref[dm_rows + 2:dm_rows + 3, :] = over_seq(s3, 2)
        o_ref[dm_rows + 3:dm_rows + 4, 0:POOL_W] = over_seq(sps, 0)
        o_ref[dm_rows + 4:dm_rows + 5, 0:QL] = over_seq(sq, 0)
        o_ref[dm_rows + 5:dm_rows + 6, 0:KVL] = over_seq(skv, 0)
        o_ref[dm_rows + 6:dm_rows + 7, 0:LANE] = over_seq(sp, 0)
        o_ref[dm_rows + 7:dm_rows + 8, 0:LANE] = over_seq(sp, 1)
        o_ref[dm_rows + 7:dm_rows + 8, LOSS_LANE:] = loss_ref[0:1, :]

    return _pcall(body, name=name, out_shape=_sds((dm_rows + RED_ROWS, SLAB_W), F32), in_specs=[VMEM_SPEC] * 9,
                  out_specs=VMEM_SPEC)(st1, st2, st3, st_fin, st_q, st_kv, st_prep, d_pool_scale, loss8)


def _small_allreduce(slab, pool, *, name, comm=None):
    rows, w = slab.shape
    dm = rows - RED_ROWS
    prow, pw = pool.shape
    crow = RED_ROWS + 2 * dm

    def body(slab_ref, pool_ref, red_ref, dmod_ref, ptot_ref, sib_slab, sib_pool, chip_slab, chip_pool, land_slab, land_pool,
             a_send, a_recv, b_send, b_recv):
        x, y, cc = _coords()
        chip = 2 * x + y
        sib = (x, y, 1 - cc)
        to_sib = [pltpu.make_async_remote_copy(src_ref=slab_ref, dst_ref=sib_slab, send_sem=a_send.at[0], recv_sem=a_recv.at[0],
                                               device_id=sib, device_id_type=MESH),
                  pltpu.make_async_remote_copy(src_ref=pool_ref, dst_ref=sib_pool, send_sem=a_send.at[1], recv_sem=a_recv.at[1],
                                               device_id=sib, device_id_type=MESH)]
        for cp in to_sib:
            cp.start()
        for cp in to_sib:
            cp.wait()
        mine_dm, theirs_dm = slab_ref[0:dm, :], sib_slab[0:dm, :]
        chip_slab[0:RED_ROWS, :] = slab_ref[dm:, :] + sib_slab[dm:, :]
        chip_slab[RED_ROWS:RED_ROWS + dm, :] = jnp.where(cc == 0, mine_dm, theirs_dm)
        chip_slab[RED_ROWS + dm:, :] = jnp.where(cc == 0, theirs_dm, mine_dm)
        chip_pool[...] = pool_ref[...] + sib_pool[...]

        sends = []
        for j, (dx, dy) in enumerate(CHIP_RELS):
            peer = (x ^ dx, y ^ dy, cc)
            sends.append(pltpu.make_async_remote_copy(src_ref=chip_slab, dst_ref=land_slab.at[chip], send_sem=b_send.at[j, 0],
                                                      recv_sem=b_recv.at[j, 0], device_id=peer, device_id_type=MESH))
            sends.append(pltpu.make_async_remote_copy(src_ref=chip_pool, dst_ref=land_pool.at[chip], send_sem=b_send.at[j, 1],
                                                      recv_sem=b_recv.at[j, 1], device_id=peer, device_id_type=MESH))
        for cp in sends:
            cp.start()
        land_slab[chip] = chip_slab[...]
        land_pool[chip] = chip_pool[...]
        for j, (dx, dy) in enumerate(CHIP_RELS):
            peer_chip = 2 * (x ^ dx) + (y ^ dy)
            pltpu.make_async_remote_copy(src_ref=chip_slab, dst_ref=land_slab.at[peer_chip], send_sem=b_send.at[j, 0],
                                         recv_sem=b_recv.at[j, 0], device_id=(x, y, cc), device_id_type=MESH).wait_recv()
            pltpu.make_async_remote_copy(src_ref=chip_pool, dst_ref=land_pool.at[peer_chip], send_sem=b_send.at[j, 1],
                                         recv_sem=b_recv.at[j, 1], device_id=(x, y, cc), device_id_type=MESH).wait_recv()
        red = land_slab[0, 0:RED_ROWS, :]
        ptot = land_pool[0]
        for ch in range(1, N_CHIP):
            red = red + land_slab[ch, 0:RED_ROWS, :]
            ptot = ptot + land_pool[ch]
        red_ref[...] = red
        ptot_ref[...] = ptot
        for ch in range(N_CHIP):
            dmod_ref[2 * dm * ch:2 * dm * (ch + 1), :] = land_slab[ch, RED_ROWS:, :]
        for cp in sends:
            cp.wait_send()

    dma = pltpu.SemaphoreType.DMA
    return _pcall(body, name=name, in_specs=[VMEM_SPEC, VMEM_SPEC], out_specs=[VMEM_SPEC] * 3,
                  out_shape=[_sds((RED_ROWS, w), F32), _sds((N_DEV * dm, w), F32), _sds((prow, pw), F32)],
                  scratch=[pltpu.VMEM((rows, w), F32), pltpu.VMEM((prow, pw), F32), pltpu.VMEM((crow, w), F32),
                           pltpu.VMEM((prow, pw), F32), pltpu.VMEM((N_CHIP, crow, w), F32), pltpu.VMEM((N_CHIP, prow, pw), F32),
                           dma((2,)), dma((2,)), dma((3, 2)), dma((3, 2))], comm=comm)(slab, pool)


def _adamw_math(w, g, m, v):
    mn = ADAM_B1 * m + (1.0 - ADAM_B1) * g
    vn = ADAM_B2 * v + (1.0 - ADAM_B2) * (g * g)
    bc1 = 1.0 / (1.0 - ADAM_B1 ** ADAM_STEP)
    bc2 = 1.0 / (1.0 - ADAM_B2 ** ADAM_STEP)
    return -ADAM_LR * ((mn * bc1) / (jnp.sqrt(vn * bc2) + ADAM_EPS) + ADAM_WD * w), mn, vn


def _small_update(red, dmod_all, pool_total, nb, params, *, name):
    names = list(SMALL)
    dm = dmod_all.shape[0] // N_DEV

    def grad_of(nm, red_ref, dmod_ref, ptot_ref):
        if nm == "b_ada":
            acc = None
            for d in range(N_DEV):
                for s in range(nb):
                    blk = dmod_ref[d * dm + 9 * s:d * dm + 9 * s + 9, :]
                    acc = blk if acc is None else acc + blk
            return jnp.concatenate([acc[k:k + 1, :] for k in range(9)], axis=1)
        if nm == "pool_grp":
            return ptot_ref[...]
        row, lo, n = {"norm_ffn1": (0, 0, D), "norm_mix": (1, 0, D), "norm_ffn2": (2, 0, D), "pool_scale": (3, 0, POOL_W),
                      "q_a_norm": (4, 0, QL), "kv_a_norm": (5, 0, KVL), "q_norm_nope": (6, 0, NOPE),
                      "q_norm_rope": (6, NOPE, ROPE), "k_norm_nope": (7, 0, NOPE), "k_norm_rope": (7, KR_LANE, ROPE)}[nm]
        return red_ref[row:row + 1, lo:lo + n]

    def body(*refs):
        red_ref, dmod_ref, ptot_ref = refs[:3]
        ins = refs[3:3 + 3 * len(names)]
        outs = refs[3 + 3 * len(names):]
        outs[4 * len(names)][...] = red_ref[RED_ROWS - 1:RED_ROWS, LOSS_LANE:]
        for i, nm in enumerate(names):
            g = grad_of(nm, red_ref, dmod_ref, ptot_ref)
            d, mn, vn = _adamw_math(ins[3 * i][...], g, ins[3 * i + 1][...], ins[3 * i + 2][...])
            outs[4 * i][...] = g
            outs[4 * i + 1][...] = d
            outs[4 * i + 2][...] = mn
            outs[4 * i + 3][...] = vn

    flat_in = [a for nm in names for a in params[nm]]
    out_shape = [_sds(params[nm][0].shape, F32) for nm in names for _ in range(4)] + [_sds((1, LANE), F32)]
    res = _pcall(body, name=name, in_specs=[VMEM_SPEC] * (3 + len(flat_in)), out_specs=[VMEM_SPEC] * len(out_shape),
                 out_shape=out_shape)(red, dmod_all, pool_total, *flat_in)
    return {nm: tuple(res[4 * i:4 * i + 4]) for i, nm in enumerate(names)}, res[-1]


def _adamw(w, g, m, v, *, name, comm=None):
    r, c = w.shape
    tr = _pick_rows(r, 256)
    tc = c if tr < r else _pick(c, 256)
    spec = pl.BlockSpec((tr, tc), lambda i, j: (i, j))
    bc1 = 1.0 / (1.0 - ADAM_B1 ** ADAM_STEP)
    bc2 = 1.0 / (1.0 - ADAM_B2 ** ADAM_STEP)

    def body(w_ref, g_ref, m_ref, v_ref, d_ref, mo_ref, vo_ref):
        gv = g_ref[...]
        mn = ADAM_B1 * m_ref[...] + (1.0 - ADAM_B1) * gv
        vn = ADAM_B2 * v_ref[...] + (1.0 - ADAM_B2) * (gv * gv)
        mo_ref[...] = mn
        vo_ref[...] = vn
        d_ref[...] = -ADAM_LR * ((mn * bc1) / (jnp.sqrt(vn * bc2) + ADAM_EPS) + ADAM_WD * w_ref[...])

    out = _sds((r, c), F32)
    return _pcall(body, name=name, out_shape=[out, out, out], grid=(r // tr, c // tc), in_specs=[spec] * 4, out_specs=[spec] * 3,
                  comm=comm)(w, g, m, v)


BIG = ("w_ffn1_in", "w_ffn1_out", "w_in", "w_pool_proj", "w_q_up", "w_kv_up", "w_mla_proj", "w_out", "w_ffn2_in", "w_ffn2_out")
ROW_SHARDED = ("w_ffn1_out", "w_out", "w_ffn2_out")
KERNEL_NAME = {"w_ffn1_in": "ffn1_in", "w_ffn1_out": "ffn1_out", "w_in": "w_in", "w_pool_proj": "pool_proj", "w_q_up": "q_up",
               "w_kv_up": "kv_up", "w_mla_proj": "mla_proj", "w_out": "w_out", "w_ffn2_in": "ffn2_in", "w_ffn2_out": "ffn2_out"}
WEIGHTS = ("w_ada", "b_ada", "norm_ffn1", "w_ffn1_in", "w_ffn1_out", "norm_mix", "w_in", "pool_grp", "pool_scale", "w_pool_proj",
           "q_a_norm", "w_q_up", "kv_a_norm", "w_kv_up", "q_norm_nope", "q_norm_rope", "k_norm_nope", "k_norm_rope", "w_mla_proj",
           "w_out", "norm_ffn2", "w_ffn2_in", "w_ffn2_out")
SMALL = ("b_ada", "norm_ffn1", "norm_mix", "pool_grp", "pool_scale", "q_a_norm", "kv_a_norm", "q_norm_nope", "q_norm_rope",
         "k_norm_nope", "k_norm_rope", "norm_ffn2")
SMALL_SEQ = tuple(n for n in SMALL if n != "pool_grp")
SLAB_W = 1024


def _assemble(name, stacked):
    if name in ROW_SHARDED:
        return stacked.reshape(stacked.shape[0] * stacked.shape[1], stacked.shape[2])
    return jnp.transpose(stacked, (1, 0, 2)).reshape(stacked.shape[1], stacked.shape[0] * stacked.shape[2])


def _split(name, full):
    if name in ROW_SHARDED:
        return full.reshape(N_CHIP, full.shape[0] // N_CHIP, full.shape[1])
    return jnp.transpose(full.reshape(full.shape[0], N_CHIP, full.shape[1] // N_CHIP), (1, 0, 2))


GU = ("w_ffn1_in", "w_ffn2_in")
NARROW = ("w_in", "w_q_up")


def _to_rows(v, width=SLAB_W):
    flat = v.reshape(-1)
    rows = -(-flat.shape[0] // width)
    return jnp.pad(flat, (0, rows * width - flat.shape[0])).reshape(rows, width)


def _pack_small(parts, names=SMALL):
    rows, spans, at = [], {}, 0
    for name in names:
        r = _to_rows(parts[name])
        spans[name] = (at, parts[name].size, parts[name].shape)
        rows.append(r)
        at += r.shape[0]
    pad = (-at) % SUBLANE
    if pad:
        rows.append(jnp.zeros((pad, SLAB_W), F32))
    return jnp.concatenate(rows, axis=0), spans


def _unpack_small(slab, spans):
    out = {}
    for name, (at, size, shape) in spans.items():
        nrow = -(-size // SLAB_W)
        out[name] = slab[at:at + nrow].reshape(-1)[:size].reshape(shape)
    return out


MIX_SMALL = ("w_out", "w_pool_proj", "w_mla_proj", "w_q_up", "w_kv_up")
RIDES = {
    "ada_prologue": (("gather", ("w_ffn1_in",)),),
    "ffn1_up_a": (("gather", ("w_ffn1_out",)),),
    "ffn1_up_b": (("gather", ("w_in",)),),
    "mix_in": (("gather", MIX_SMALL),),
    "attn_fwd": (("gather", ("w_ffn2_in", "w_ffn2_out")),),
    "d_ffn2_h": (("swap", ("w_ffn2_out", "w_ffn2_in")),),
    "attn_bwd": (("exchange", ("w_ffn2_out", "w_ffn2_in")),),
    "d_mix_in": (("swap", MIX_SMALL),),
    "d_mix_h": (("exchange", MIX_SMALL), ("swap", ("w_in",))),
    "d_ffn1_out": (("exchange", ("w_in",)),),
    "ffn1_dact": (("swap", ("w_ffn1_out",)),),
    "d_ffn1_in_a": (("exchange", ("w_ffn1_out",)),),
    "d_ffn1_in_b": (("swap", ("w_ffn1_in@0",)),),
    "d_ffn1_h": (("exchange", ("w_ffn1_in@0",)), ("swap", ("w_ffn1_in@1",))),
    "bwd_norm1": (("exchange", ("w_ffn1_in@1",)),),
    "small_allreduce": (("join", tuple(n for n in BIG if n != "w_ffn1_in") + ("w_ffn1_in@0", "w_ffn1_in@1")),),
}


def _kname(n):
    base, _, part = n.partition("@")
    return KERNEL_NAME[base] + ("@" + part if part else "")


def _both(comms):
    if len(comms) == 1:
        return comms[0]
    ins, outs, sems, aliases, spans = [], [], [], {}, []
    for c in comms:
        spans.append((len(ins), len(c.ins), len(outs), len(c.out_shapes), len(sems), len(c.sems)))
        aliases.update({len(ins) + i: len(outs) + o for i, o in c.aliases.items()})
        ins, outs, sems = ins + c.ins, outs + c.out_shapes, sems + c.sems

    def each(which):
        def run(i_, o_, s_):
            for c, (ia, ni, oa, no, sa, ns) in zip(comms, spans):
                getattr(c, which)(i_[ia:ia + ni], o_[oa:oa + no], s_[sa:sa + ns])
        return run

    return _Comm(ins, outs, sems, each("start"), each("finish"), aliases)


class _ExchangePlan:
    def __init__(self, shards, cidx):
        self.shards, self.cidx = shards, cidx
        self.W, self.G, self.parts, self.pre, self.reduced, self.joined = {}, {}, {}, {}, {}, {}

    def grad(self, key, g):
        self.G[key] = g

    def rider(self, name):
        comms = []
        for kind, names in RIDES.get(name, ()):
            if kind.startswith("gather"):
                comms.append(_gather_comm([self.shards[n] for n in names], only_y={"gather_lo": 0, "gather_hi": 1}.get(kind)))
            elif kind == "swap":
                for n in names:
                    self.parts[n] = self._stacked(n)
                comms.append(_swap_comm([self.parts[n] for n in names]))
            elif kind == "exchange":
                comms.append(_exchange_comm([self.pre[n] for n in names]))
            else:
                comms.append(_join_comm([self.reduced[n] for n in names]))
        return _both(comms) if comms else None

    def landed(self, name, outs):
        at = 0
        for kind, names in RIDES[name]:
            for n, o in zip(names, outs[at:at + len(names)]):
                if kind.startswith("gather"):
                    self.W[KERNEL_NAME[n] + kind[len("gather"):]] = self._to_kernel(n, o)
                elif kind == "swap":
                    self.pre[n] = _add_half(self.parts[n], o, self.cidx, name="rs_add_" + _kname(n))
                elif kind == "exchange":
                    self.reduced[n] = _sum_chips(o, self.cidx, name="rs_sum_" + _kname(n))
                else:
                    self.joined[n] = o
            at += len(names)

    @staticmethod
    def _to_kernel(n, stacked):
        if n in GU:
            return stacked
        full = _assemble(n, stacked)
        return {"w_in": _w_in_to_kernel, "w_q_up": _q_up_to_kernel, "w_kv_up": _kv_up_to_kernel}.get(n, lambda w: w)(full)

    def _stacked(self, n):
        g = self.G[_kname(n)]
        if n.partition("@")[0] in GU:
            return g
        full = {"w_in": _w_in_from_kernel, "w_q_up": _q_up_from_kernel, "w_kv_up": _kv_up_from_kernel}.get(n, lambda w: w)(g)
        return _split(n, full)


def kernel(x, c, positions, w_ada, b_ada, norm_ffn1, w_ffn1_in, w_ffn1_out, norm_mix, w_in, pool_grp, pool_scale, w_pool_proj, q_a_norm, w_q_up, kv_a_norm, w_kv_up, q_norm_nope, q_norm_rope, k_norm_nope, k_norm_rope, w_mla_proj, w_out, norm_ffn2, w_ffn2_in, w_ffn2_out, loss_target, m_w_ada, m_b_ada, m_norm_ffn1, m_w_ffn1_in, m_w_ffn1_out, m_norm_mix, m_w_in, m_pool_grp, m_pool_scale, m_w_pool_proj, m_q_a_norm, m_w_q_up, m_kv_a_norm, m_w_kv_up, m_q_norm_nope, m_q_norm_rope, m_k_norm_nope, m_k_norm_rope, m_w_mla_proj, m_w_out, m_norm_ffn2, m_w_ffn2_in, m_w_ffn2_out, v_w_ada, v_b_ada, v_norm_ffn1, v_w_ffn1_in, v_w_ffn1_out, v_norm_mix, v_w_in, v_pool_grp, v_pool_scale, v_w_pool_proj, v_q_a_norm, v_w_q_up, v_kv_a_norm, v_w_kv_up, v_q_norm_nope, v_q_norm_rope, v_k_norm_nope, v_k_norm_rope, v_w_mla_proj, v_w_out, v_norm_ffn2, v_w_ffn2_in, v_w_ffn2_out):
    args = dict(locals())
    wts = {n: args[n][0] for n in WEIGHTS}
    mom = {n: args["m_" + n][0] for n in WEIGHTS}
    var = {n: args["v_" + n][0] for n in WEIGHTS}
    nb, seq, dm = x.shape
    tokens = nb * seq
    xi, yi, ci = _coords()
    chip = 2 * xi + yi
    dev = 4 * xi + 2 * yi + ci

    cidx = ci.astype(jnp.int32).reshape(1)
    plan = _ExchangePlan({n: wts[n].astype(MX) for n in BIG}, cidx)
    plan.W["pool_grp"] = wts["pool_grp"].astype(MX)

    ncol = w_ada.shape[2]
    b_cols = lax.dynamic_slice_in_dim(wts["b_ada"].reshape(1, -1), chip * ncol, ncol, axis=1)
    c_slots, mod_slots = _run(plan, _ada_prologue, c, wts["w_ada"], b_cols, name="ada_prologue")
    plan.W["ffn1_in_lo"] = plan.W["ffn1_in_hi"] = plan.W["ffn1_in"]
    c_all = c_slots[:, :nb].reshape(N_DEV * nb, dm)
    mod3 = jnp.transpose(mod_slots[:, :nb], (1, 0, 2)).reshape(nb, 9, dm)
    P = {"norm_ffn1": wts["norm_ffn1"].reshape(1, dm), "norm_mix": wts["norm_mix"].reshape(1, dm),
         "norm_ffn2": wts["norm_ffn2"].reshape(1, dm), "pool_scale": wts["pool_scale"].reshape(1, POOL_W),
         "q_a_norm": wts["q_a_norm"].reshape(1, QL), "kv_a_norm": wts["kv_a_norm"].reshape(1, KVL),
         "q_gain": _gain_slab(wts["q_norm_nope"].reshape(1, NOPE), wts["q_norm_rope"].reshape(1, ROPE)),
         "k_gain": _gain_slab(wts["k_norm_nope"].reshape(1, NOPE), wts["k_norm_rope"].reshape(1, ROPE))}
    cos, sin = _rope_tables(positions.reshape(tokens))

    loss8, grad_x, stats, d_pool_grp = _layer_fwd_bwd(x.reshape(tokens, dm), loss_target.reshape(tokens, dm), mod3, cos, sin, plan, P)

    slab = _pack_stats(*stats, loss8, name="pack_stats")
    red, dmod_rows, pool_total = _run(plan, _small_allreduce, slab, d_pool_grp.reshape(POOL_G * LANE, LANE), name="small_allreduce")
    grads = {n: plan.joined[n] for n in BIG if n != "w_ffn1_in"}
    grads["w_ffn1_in"] = jnp.concatenate([plan.joined["w_ffn1_in@0"], plan.joined["w_ffn1_in@1"]], axis=0)
    dm_rows = dmod_rows.shape[0] // N_DEV
    dmod_all = dmod_rows.reshape(N_DEV, dm_rows, SLAB_W)[:, :9 * nb].reshape(N_DEV * nb, 9 * dm)
    dmod_cols = lax.dynamic_slice_in_dim(dmod_all, chip * ncol, ncol, axis=1)
    grads["w_ada"] = _ada_bwd(c_all, dmod_cols, name="ada_bwd")

    delta, new_m, new_v = {}, {}, {}
    as2d = lambda a: a.reshape(POOL_G * LANE, LANE) if a.ndim == 4 else a.reshape(1, -1)
    upd, loss_row = _small_update(red, dmod_rows, pool_total, nb,
                                  {n: tuple(as2d(args[p + n]) for p in ("", "m_", "v_")) for n in SMALL}, name="small_update")
    loss = loss_row[0, 0]
    for n in SMALL:
        grads[n], delta[n], new_m[n], new_v[n] = upd[n]
    for n in ("w_ada",) + BIG:
        if n in NARROW:
            res = _adamw(wts[n].T, grads[n].T, mom[n].T, var[n].T, name="adamw_" + n)
            delta[n], new_m[n], new_v[n] = (r.T for r in res)
        else:
            delta[n], new_m[n], new_v[n] = _adamw(wts[n], grads[n], mom[n], var[n], name="adamw_" + n)

    def lead(a, n):
        return a.reshape((1,) + wts[n].shape)

    return (loss, grad_x.reshape(nb, seq, dm), *[lead(grads[n], n) for n in WEIGHTS], *[lead(delta[n], n) for n in WEIGHTS],
            *[lead(new_m[n], n) for n in WEIGHTS], *[lead(new_v[n], n) for n in WEIGHTS])
```

```python
import functools
import math

import jax
import jax.numpy as jnp
from jax import lax
from jax.experimental import pallas as pl
from jax.experimental.pallas import tpu as pltpu

F32 = jnp.float32
MX = jnp.bfloat16

D = 1024
DFF = 2816
NH = 8
POOL_W = 512
POOL_G = 4
QL = 384
KVL = 256
ROPE = 32
NOPE = 64
LANE = 128
SUBLANE = 8
EPS = 1e-6
ATTN_SCALE = 1.0 / math.sqrt(96.0)
NEG = -1e30

Z_UP, Z_QL, Z_KV, Z_KR, Z_GP, Z_GM, Z_W = 0, 512, 896, 1152, 1280, 2304, 3328
KR_LANE = 64
LAT_W = 1280

ADAM_LR, ADAM_B1, ADAM_B2, ADAM_EPS, ADAM_WD, ADAM_STEP = 0.001, 0.9, 0.999, 1e-08, 0.01, 10

VMEM_LIMIT = 48 * 1024 * 1024
MESH = pl.DeviceIdType.MESH
N_DEV = 8
N_CHIP = 4


class _Comm:
    def __init__(self, ins, out_shapes, sems, start, finish, aliases=None):
        self.ins, self.out_shapes, self.sems = list(ins), list(out_shapes), list(sems)
        self.start, self.finish = start, finish
        self.aliases = aliases or {}


def _pcall(body, *, name, out_shape, grid=(), in_specs=None, out_specs=None, scratch=(), grid_spec=None, aliases=None,
           comm=None):
    params = pltpu.CompilerParams(vmem_limit_bytes=VMEM_LIMIT)
    kw = dict(name=name, compiler_params=params)
    if comm is None:
        if aliases:
            kw["input_output_aliases"] = aliases
        if grid_spec is not None:
            return pl.pallas_call(body, grid_spec=grid_spec, out_shape=out_shape, **kw)
        return pl.pallas_call(body, grid=grid, in_specs=in_specs, out_specs=out_specs, scratch_shapes=scratch,
                              out_shape=out_shape, **kw)
    single = not isinstance(out_shape, (list, tuple))
    outs = [out_shape] if single else list(out_shape)
    ospecs = [out_specs] if single else list(out_specs)
    n_in, n_out, n_ci, n_co, n_scr = len(in_specs), len(outs), len(comm.ins), len(comm.out_shapes), len(scratch)
    io = dict(aliases or {})
    io.update({n_in + i: n_out + o for i, o in comm.aliases.items()})

    def riding(*refs):
        ins, cins = refs[:n_in], refs[n_in:n_in + n_ci]
        at = n_in + n_ci
        os_, couts = refs[at:at + n_out], refs[at + n_out:at + n_out + n_co]
        at += n_out + n_co
        scr, csems = refs[at:at + n_scr], refs[at + n_scr:]
        if grid:
            first = functools.reduce(jnp.logical_and, [pl.program_id(d) == 0 for d in range(len(grid))])
            last = functools.reduce(jnp.logical_and, [pl.program_id(d) == grid[d] - 1 for d in range(len(grid))])
            pl.when(first)(lambda: comm.start(cins, couts, csems))
            body(*ins, *os_, *scr)
            pl.when(last)(lambda: comm.finish(cins, couts, csems))
        else:
            comm.start(cins, couts, csems)
            body(*ins, *os_, *scr)
            comm.finish(cins, couts, csems)

    call = pl.pallas_call(riding, grid=grid, in_specs=list(in_specs) + [HBM_SPEC] * n_ci, out_specs=ospecs + [HBM_SPEC] * n_co,
                          out_shape=outs + comm.out_shapes, scratch_shapes=list(scratch) + comm.sems,
                          input_output_aliases=io, **kw)

    def run(*args):
        res = call(*args, *comm.ins)
        main = list(res[:n_out])
        return (main[0] if single else main), list(res[n_out:])

    return run


def _pick(dim, target):
    best = None
    for t in range(LANE, min(dim, target) + 1, LANE):
        if dim % t == 0:
            best = t
    return dim if best is None else best


def _sds(shape, dtype):
    return jax.ShapeDtypeStruct(shape, dtype)


def _dw(a, g, *, name, tm=512, tn=1024, comm=None):
    return _mm(a, g, mode="tn", out_dtype=F32, name=name, tm=tm, tn=tn, tk=a.shape[0], n_outer=True, comm=comm)


def _mm(a, b, *, mode, out_dtype, name, tm=1024, tn=1024, tk=4096, n_outer=False, comm=None):
    if mode == "nn":
        (M, K), (K2, N) = a.shape, b.shape
    elif mode == "nt":
        (M, K), (N, K2) = a.shape, b.shape
    else:
        (K, M), (K2, N) = a.shape, b.shape
    assert K == K2, (name, a.shape, b.shape)
    tm, tn, tk = _pick(M, tm), _pick(N, tn), _pick(K, tk)
    nk = K // tk
    if n_outer:
        ij = lambda g0, g1: (g1, g0)
        grid = (N // tn, M // tm, nk)
    else:
        ij = lambda g0, g1: (g0, g1)
        grid = (M // tm, N // tn, nk)
    if mode == "tn":
        a_spec = pl.BlockSpec((tk, tm), lambda g0, g1, k: (k, ij(g0, g1)[0]))
    else:
        a_spec = pl.BlockSpec((tm, tk), lambda g0, g1, k: (ij(g0, g1)[0], k))
    if mode == "nt":
        b_spec = pl.BlockSpec((tn, tk), lambda g0, g1, k: (ij(g0, g1)[1], k))
    else:
        b_spec = pl.BlockSpec((tk, tn), lambda g0, g1, k: (k, ij(g0, g1)[1]))
    o_spec = pl.BlockSpec((tm, tn), lambda g0, g1, k: ij(g0, g1))
    dn = {"nn": (((1,), (0,)), ((), ())), "nt": (((1,), (1,)), ((), ())), "tn": (((0,), (0,)), ((), ()))}[mode]

    def dot(a_ref, b_ref):
        return lax.dot_general(a_ref[...].astype(MX), b_ref[...].astype(MX), dn, preferred_element_type=F32)

    def body_one(a_ref, b_ref, o_ref):
        o_ref[...] = dot(a_ref, b_ref).astype(o_ref.dtype)

    def body_acc(a_ref, b_ref, o_ref, acc_ref):
        k = pl.program_id(2)
        part = dot(a_ref, b_ref)

        @pl.when(k == 0)
        def _():
            acc_ref[...] = part

        @pl.when(k > 0)
        def _():
            acc_ref[...] += part

        @pl.when(k == nk - 1)
        def _():
            o_ref[...] = acc_ref[...].astype(o_ref.dtype)

    return _pcall(body_one if nk == 1 else body_acc, name=name, out_shape=_sds((M, N), out_dtype), grid=grid,
                  in_specs=[a_spec, b_spec], out_specs=o_spec, scratch=[] if nk == 1 else [pltpu.VMEM((tm, tn), F32)],
                  comm=comm)(a, b)


def _gu_shard(q):
    return (q % 2) * 2 + q // 2


def _ffn_up(h, w_st, *, name, tm=512, comm=None):
    T, dm = h.shape
    hw = w_st.shape[2]
    tm = _pick(T, tm)

    def body(h_ref, wg_ref, wu_ref, gu_ref, a_ref):
        hv = h_ref[...]
        g = jnp.dot(hv, wg_ref[0], preferred_element_type=F32)
        u = jnp.dot(hv, wu_ref[0], preferred_element_type=F32)
        gu_ref[:, :hw] = g.astype(gu_ref.dtype)
        gu_ref[:, hw:] = u.astype(gu_ref.dtype)
        a_ref[...] = (g * _sigmoid(g) * u).astype(a_ref.dtype)

    return _pcall(body, name=name, grid=(T // tm, 2),
                  in_specs=[pl.BlockSpec((tm, dm), lambda i, j: (i, 0)), pl.BlockSpec((1, dm, hw), lambda i, j: (j, 0, 0)),
                            pl.BlockSpec((1, dm, hw), lambda i, j: (2 + j, 0, 0))],
                  out_specs=[pl.BlockSpec((tm, 2 * hw), lambda i, j: (i, j)), pl.BlockSpec((tm, hw), lambda i, j: (i, j))],
                  out_shape=[_sds((T, 4 * hw), MX), _sds((T, 2 * hw), MX)], comm=comm)(h, w_st, w_st)


def _ffn_up_first(x, mod3, gain, w_lo, *, sub, name, tm=512, comm=None):
    T, dm = x.shape
    hw = w_lo.shape[2]
    tm = _pick(T, tm)
    tps = (T // mod3.shape[0]) // tm

    def body(x_ref, mod_ref, n_ref, wg_ref, wu_ref, h_ref, gu_ref, a_ref):
        xv = x_ref[...]
        xn = xv * _rsq(xv) * n_ref[...]
        hv = (xn * (1.0 + mod_ref[0, 3 * sub + 1:3 * sub + 2, :]) + mod_ref[0, 3 * sub:3 * sub + 1, :]).astype(h_ref.dtype)
        h_ref[...] = hv
        g = jnp.dot(hv, wg_ref[0], preferred_element_type=F32)
        u = jnp.dot(hv, wu_ref[0], preferred_element_type=F32)
        gu_ref[:, :hw] = g.astype(gu_ref.dtype)
        gu_ref[:, hw:] = u.astype(gu_ref.dtype)
        a_ref[...] = (g * _sigmoid(g) * u).astype(a_ref.dtype)

    return _pcall(body, name=name, grid=(T // tm,),
                  in_specs=[_row_spec(tm, dm), pl.BlockSpec((1, 9, dm), lambda i: (i // tps, 0, 0)),
                            pl.BlockSpec((1, dm), lambda i: (0, 0)), pl.BlockSpec((1, dm, hw), lambda i: (0, 0, 0)),
                            pl.BlockSpec((1, dm, hw), lambda i: (2, 0, 0))],
                  out_specs=[_row_spec(tm, dm), pl.BlockSpec((tm, 2 * hw), lambda i: (i, 0)), pl.BlockSpec((tm, hw), lambda i: (i, 0))],
                  out_shape=[_sds((T, dm), MX), _sds((T, 4 * hw), MX), _sds((T, 2 * hw), MX)],
                  comm=comm)(x, mod3, gain, w_lo, w_lo)


def _ffn_up_second(h, w_hi, gu, a, *, name, tm=512, comm=None):
    T, dm = h.shape
    hw = w_hi.shape[2]
    tm = _pick(T, tm)

    def body(h_ref, wg_ref, wu_ref, gu_in, a_in, gu_ref, a_ref):
        hv = h_ref[...]
        g = jnp.dot(hv, wg_ref[0], preferred_element_type=F32)
        u = jnp.dot(hv, wu_ref[0], preferred_element_type=F32)
        gu_ref[:, :hw] = g.astype(gu_ref.dtype)
        gu_ref[:, hw:] = u.astype(gu_ref.dtype)
        a_ref[...] = (g * _sigmoid(g) * u).astype(a_ref.dtype)

    return _pcall(body, name=name, grid=(T // tm,),
                  in_specs=[_row_spec(tm, dm), pl.BlockSpec((1, dm, hw), lambda i: (1, 0, 0)),
                            pl.BlockSpec((1, dm, hw), lambda i: (3, 0, 0)), HBM_SPEC, HBM_SPEC],
                  out_specs=[pl.BlockSpec((tm, 2 * hw), lambda i: (i, 1)), pl.BlockSpec((tm, hw), lambda i: (i, 1))],
                  out_shape=[_sds(gu.shape, gu.dtype), _sds(a.shape, a.dtype)], aliases={3: 0, 4: 1},
                  comm=comm)(h, w_hi, w_hi, gu, a)


def _ffn_dact(df, gu, w_out, *, name, tm=512, comm=None):
    T, dm = df.shape
    hw = gu.shape[1] // 4
    tm = _pick(T, tm)

    def body(df_ref, gu_ref, wo_ref, dgu_ref):
        da = lax.dot_general(df_ref[...], wo_ref[...], (((1,), (1,)), ((), ())), preferred_element_type=F32)
        g = gu_ref[:, :hw].astype(F32)
        u = gu_ref[:, hw:].astype(F32)
        s = _sigmoid(g)
        dgu_ref[:, :hw] = (da * u * (s * (1.0 + g * (1.0 - s)))).astype(dgu_ref.dtype)
        dgu_ref[:, hw:] = (da * (g * s)).astype(dgu_ref.dtype)

    return _pcall(body, name=name, grid=(T // tm, 2),
                  in_specs=[pl.BlockSpec((tm, dm), lambda i, j: (i, 0)), pl.BlockSpec((tm, 2 * hw), lambda i, j: (i, j)),
                            pl.BlockSpec((hw, dm), lambda i, j: (j, 0))],
                  out_specs=pl.BlockSpec((tm, 2 * hw), lambda i, j: (i, j)), out_shape=_sds(gu.shape, MX),
                  comm=comm)(df, gu, w_out)


def _ffn_dh(dgu, w_lo, w_hi, *, name, tm=1024, comm=None):
    T = dgu.shape[0]
    _, dm, hw = w_lo.shape
    tm = _pick(T, tm)

    def body(d_ref, lo_ref, hi_ref, o_ref, acc_ref):
        q = pl.program_id(1)

        def part(w_ref):
            return lax.dot_general(d_ref[...], w_ref[0], (((1,), (1,)), ((), ())), preferred_element_type=F32)

        @pl.when(q == 0)
        def _():
            acc_ref[...] = part(lo_ref)

        @pl.when(q == 1)
        def _():
            acc_ref[...] += part(lo_ref)

        @pl.when(q == 2)
        def _():
            acc_ref[...] += part(hi_ref)

        @pl.when(q == 3)
        def _():
            o_ref[...] = acc_ref[...] + part(hi_ref)

    return _pcall(body, name=name, grid=(T // tm, 4),
                  in_specs=[pl.BlockSpec((tm, hw), lambda i, q: (i, q)),
                            pl.BlockSpec((1, dm, hw), lambda i, q: (jnp.where(q < 2, _gu_shard(q), 2), 0, 0)),
                            pl.BlockSpec((1, dm, hw), lambda i, q: (jnp.where(q < 2, 1, _gu_shard(q)), 0, 0))],
                  out_specs=pl.BlockSpec((tm, dm), lambda i, q: (i, 0)), out_shape=_sds((T, dm), F32),
                  scratch=[pltpu.VMEM((tm, dm), F32)], comm=comm)(dgu, w_lo, w_hi)


def _ffn_dw_in(h, dgu, *, name, rows=None, tm=512, comm=None):
    T, dm = h.shape
    hw = dgu.shape[1] // 4
    first, count = rows if rows is not None else (0, dm)
    tm = _pick(count, tm)
    skip = first // tm

    def body(h_ref, d_ref, o_ref):
        o_ref[0] = lax.dot_general(h_ref[...], d_ref[...], (((0,), (0,)), ((), ())), preferred_element_type=F32)

    return _pcall(body, name=name, grid=(4, count // tm),
                  in_specs=[pl.BlockSpec((T, tm), lambda q, i: (0, skip + i)), pl.BlockSpec((T, hw), lambda q, i: (0, q))],
                  out_specs=pl.BlockSpec((1, tm, hw), lambda q, i: (_gu_shard(q), i, 0)),
                  out_shape=_sds((4, count, hw), F32), comm=comm)(h, dgu)


def _rsq(x):
    return lax.rsqrt(jnp.mean(x * x, axis=-1, keepdims=True) + EPS)


def _sigmoid(x):
    return 1.0 / (1.0 + jnp.exp(-x))


def _row_spec(tm, w):
    return pl.BlockSpec((tm, w), lambda i: (i, 0))


def _fwd_block(x_prev, f_prev, mod3, gain, *, sub, coef, name, tm=256):
    T, dm = x_prev.shape
    tps = (T // mod3.shape[0]) // tm
    has_f = f_prev is not None
    mod_spec = pl.BlockSpec((1, 9, dm), lambda i: (i // tps, 0, 0))
    vec_spec = pl.BlockSpec((1, dm), lambda i: (0, 0))

    def body(*refs):
        if has_f:
            x_ref, f_ref, mod_ref, n_ref, xo_ref, h_ref = refs
            x = x_ref[...] + coef * mod_ref[0, 3 * sub - 1:3 * sub, :] * f_ref[...]
            xo_ref[...] = x
        else:
            x_ref, mod_ref, n_ref, h_ref = refs
            x = x_ref[...]
        xn = x * _rsq(x) * n_ref[...]
        h = xn * (1.0 + mod_ref[0, 3 * sub + 1:3 * sub + 2, :]) + mod_ref[0, 3 * sub:3 * sub + 1, :]
        h_ref[...] = h.astype(h_ref.dtype)

    row = _row_spec(tm, dm)
    if has_f:
        return _pcall(body, name=name, grid=(T // tm,), in_specs=[row, row, mod_spec, vec_spec], out_specs=[row, row],
                      out_shape=[_sds((T, dm), F32), _sds((T, dm), MX)])(x_prev, f_prev, mod3, gain)
    return _pcall(body, name=name, grid=(T // tm,), in_specs=[row, mod_spec, vec_spec], out_specs=row,
                  out_shape=_sds((T, dm), MX))(x_prev, mod3, gain)


def _mm_resid_norm(a, w, x_prev, mod3, gain, *, sub, coef, name, tm=512, comm=None):
    T, k = a.shape
    dm = w.shape[1]
    tm = _pick(T, tm)
    tps = (T // mod3.shape[0]) // tm

    def body(a_ref, w_ref, x_ref, mod_ref, n_ref, f_ref, xo_ref, h_ref):
        f = jnp.dot(a_ref[...], w_ref[...], preferred_element_type=F32)
        f_ref[...] = f
        x = x_ref[...] + coef * mod_ref[0, 3 * sub - 1:3 * sub, :] * f
        xo_ref[...] = x
        xn = x * _rsq(x) * n_ref[...]
        h_ref[...] = (xn * (1.0 + mod_ref[0, 3 * sub + 1:3 * sub + 2, :]) + mod_ref[0, 3 * sub:3 * sub + 1, :]).astype(h_ref.dtype)

    row = _row_spec(tm, dm)
    return _pcall(body, name=name, grid=(T // tm,),
                  in_specs=[_row_spec(tm, k), pl.BlockSpec((k, dm), lambda i: (0, 0)), row,
                            pl.BlockSpec((1, 9, dm), lambda i: (i // tps, 0, 0)), pl.BlockSpec((1, dm), lambda i: (0, 0))],
                  out_specs=[row, row, row], out_shape=[_sds((T, dm), F32), _sds((T, dm), F32), _sds((T, dm), MX)],
                  comm=comm)(a, w, x_prev, mod3, gain)


def _mm_loss(a, w, x2, tgt, mod3, *, name, tm=512):
    T, k = a.shape
    dm = w.shape[1]
    tm = _pick(T, tm)
    tps = (T // mod3.shape[0]) // tm
    mod_spec = pl.BlockSpec((1, 9, dm), lambda i: (i // tps, 0, 0))
    row = _row_spec(tm, dm)
    stat_spec = pl.BlockSpec((1, SUBLANE, dm), lambda i: (i // tps, 0, 0))
    loss_spec = pl.BlockSpec((SUBLANE, LANE), lambda i: (0, 0))

    def body(a_ref, w_ref, x_ref, t_ref, mod_ref, dy_ref, df_ref, st_ref, loss_ref):
        i = pl.program_id(0)
        g = mod_ref[0, 8:9, :]
        f = jnp.dot(a_ref[...], w_ref[...], preferred_element_type=F32)
        err = x_ref[...] + 0.5 * g * f - t_ref[...]
        dy = err * (1.0 / dm)
        dy_ref[...] = dy
        df_ref[...] = (0.5 * g * dy).astype(df_ref.dtype)
        dgate = jnp.sum(0.5 * dy * f, axis=0, keepdims=True)
        part = 0.5 * jnp.sum(jnp.sum(err * err, axis=0, keepdims=True), axis=1, keepdims=True) * (1.0 / dm)

        @pl.when(i % tps == 0)
        def _():
            st_ref[...] = jnp.zeros_like(st_ref)

        @pl.when(i == 0)
        def _():
            loss_ref[...] = jnp.zeros_like(loss_ref)

        st_ref[0, 0:1, :] += dgate
        loss_ref[...] += jnp.broadcast_to(part, loss_ref.shape)

    return _pcall(body, name=name, grid=(T // tm,),
                  in_specs=[_row_spec(tm, k), pl.BlockSpec((k, dm), lambda i: (0, 0)), row, row, mod_spec],
                  out_specs=[row, row, stat_spec, loss_spec],
                  out_shape=[_sds((T, dm), F32), _sds((T, dm), MX), _sds((mod3.shape[0], SUBLANE, dm), F32),
                             _sds((SUBLANE, LANE), F32)])(a, w, x2, tgt, mod3)


def _norm_bwd_tail(dhv, x_ref, dxi_ref, f_ref, mod_ref, n_ref, dx_ref, df_ref, st_ref, *, first, sub, coef):
    x = x_ref[...]
    r = _rsq(x)
    xhat = x * r
    n = n_ref[...]
    d_shift = jnp.sum(dhv, axis=0, keepdims=True)
    d_scale = jnp.sum(dhv * (xhat * n), axis=0, keepdims=True)
    dxn = dhv * (1.0 + mod_ref[0, 3 * sub + 1:3 * sub + 2, :])
    d_gain = jnp.sum(dxn * xhat, axis=0, keepdims=True)
    dxhat = dxn * n
    dx = dxi_ref[...] + r * (dxhat - xhat * jnp.mean(dxhat * xhat, axis=-1, keepdims=True))
    dx_ref[...] = dx

    @pl.when(first)
    def _():
        st_ref[...] = jnp.zeros_like(st_ref)

    st_ref[0, 0:1, :] += d_shift
    st_ref[0, 1:2, :] += d_scale
    st_ref[0, 2:3, :] += d_gain
    if f_ref is not None:
        st_ref[0, 3:4, :] += jnp.sum(coef * dx * f_ref[...], axis=0, keepdims=True)
        df_ref[...] = (coef * mod_ref[0, 3 * sub - 1:3 * sub, :] * dx).astype(df_ref.dtype)


def _ffn_dh_norm(dgu, w_st, x_cur, dx_in, f_prev, mod3, gain, *, sub, coef, name, tm=512, comm=None):
    T = dgu.shape[0]
    _, dm, hw = w_st.shape
    tm = _pick(T, tm)
    nb = mod3.shape[0]
    tps = (T // nb) // tm

    def body(d_ref, w_ref, x_ref, dxi_ref, f_ref, mod_ref, n_ref, dx_ref, df_ref, st_ref, acc_ref):
        i = pl.program_id(0)
        q = pl.program_id(1)
        part = lax.dot_general(d_ref[...], w_ref[0], (((1,), (1,)), ((), ())), preferred_element_type=F32)

        @pl.when(q == 0)
        def _():
            acc_ref[...] = part

        @pl.when(q > 0)
        def _():
            acc_ref[...] += part

        @pl.when(q == 3)
        def _():
            _norm_bwd_tail(acc_ref[...], x_ref, dxi_ref, f_ref, mod_ref, n_ref, dx_ref, df_ref, st_ref,
                           first=i % tps == 0, sub=sub, coef=coef)

    row = pl.BlockSpec((tm, dm), lambda i, q: (i, 0))
    return _pcall(body, name=name, grid=(T // tm, 4),
                  in_specs=[pl.BlockSpec((tm, hw), lambda i, q: (i, q)), pl.BlockSpec((1, dm, hw), lambda i, q: (_gu_shard(q), 0, 0)),
                            row, row, row, pl.BlockSpec((1, 9, dm), lambda i, q: (i // tps, 0, 0)),
                            pl.BlockSpec((1, dm), lambda i, q: (0, 0))],
                  out_specs=[row, row, pl.BlockSpec((1, SUBLANE, dm), lambda i, q: (i // tps, 0, 0))],
                  out_shape=[_sds((T, dm), F32), _sds((T, dm), MX), _sds((nb, SUBLANE, dm), F32)],
                  scratch=[pltpu.VMEM((tm, dm), F32)], comm=comm)(dgu, w_st, x_cur, dx_in, f_prev, mod3, gain)


def _mm_nt_norm(a, b, x_cur, dx_in, f_prev, mod3, gain, *, sub, coef, name, tm=512, comm=None):
    T, k = a.shape
    dm = b.shape[0]
    tm = _pick(T, tm)
    nb = mod3.shape[0]
    tps = (T // nb) // tm

    def body(a_ref, b_ref, x_ref, dxi_ref, f_ref, mod_ref, n_ref, dx_ref, df_ref, st_ref):
        dh = lax.dot_general(a_ref[...], b_ref[...], (((1,), (1,)), ((), ())), preferred_element_type=F32)
        _norm_bwd_tail(dh, x_ref, dxi_ref, f_ref, mod_ref, n_ref, dx_ref, df_ref, st_ref,
                       first=pl.program_id(0) % tps == 0, sub=sub, coef=coef)

    row = _row_spec(tm, dm)
    return _pcall(body, name=name, grid=(T // tm,),
                  in_specs=[_row_spec(tm, k), pl.BlockSpec((dm, k), lambda i: (0, 0)), row, row, row,
                            pl.BlockSpec((1, 9, dm), lambda i: (i // tps, 0, 0)), pl.BlockSpec((1, dm), lambda i: (0, 0))],
                  out_specs=[row, row, pl.BlockSpec((1, SUBLANE, dm), lambda i: (i // tps, 0, 0))],
                  out_shape=[_sds((T, dm), F32), _sds((T, dm), MX), _sds((nb, SUBLANE, dm), F32)],
                  comm=comm)(a, b, x_cur, dx_in, f_prev, mod3, gain)


def _bwd_block(x_cur, dh, dx_in, f_prev, mod3, gain, *, sub, coef, name, tm=256, comm=None):
    T, dm = x_cur.shape
    nb = mod3.shape[0]
    tps = (T // nb) // tm
    has_f = f_prev is not None
    mod_spec = pl.BlockSpec((1, 9, dm), lambda i: (i // tps, 0, 0))
    vec_spec = pl.BlockSpec((1, dm), lambda i: (0, 0))
    stat_spec = pl.BlockSpec((1, SUBLANE, dm), lambda i: (i // tps, 0, 0))
    row = _row_spec(tm, dm)

    def body(*refs):
        if has_f:
            x_ref, dh_ref, dxi_ref, f_ref, mod_ref, n_ref, dx_ref, df_ref, st_ref = refs
        else:
            x_ref, dh_ref, dxi_ref, mod_ref, n_ref, dx_ref, st_ref = refs
            f_ref = df_ref = None
        _norm_bwd_tail(dh_ref[...], x_ref, dxi_ref, f_ref, mod_ref, n_ref, dx_ref, df_ref, st_ref,
                       first=pl.program_id(0) % tps == 0, sub=sub, coef=coef)

    st_shape = _sds((nb, SUBLANE, dm), F32)
    if has_f:
        return _pcall(body, name=name, grid=(T // tm,), in_specs=[row, row, row, row, mod_spec, vec_spec],
                      out_specs=[row, row, stat_spec],
                      out_shape=[_sds((T, dm), F32), _sds((T, dm), MX), st_shape], comm=comm)(x_cur, dh, dx_in, f_prev, mod3, gain)
    return _pcall(body, name=name, grid=(T // tm,), in_specs=[row, row, row, mod_spec, vec_spec],
                  out_specs=[row, stat_spec], out_shape=[_sds((T, dm), F32), st_shape], comm=comm)(x_cur, dh, dx_in, mod3, gain)


def _merge_fwd(z, br_pool, br_mla, *, name, tm=256):
    T = z.shape[0]

    def body(z_ref, bp_ref, bm_ref, o_ref):
        gp = z_ref[:, Z_GP:Z_GP + D]
        gm = z_ref[:, Z_GM:Z_GM + D]
        o_ref[...] = (_sigmoid(gp) * bp_ref[...] + _sigmoid(gm) * bm_ref[...]).astype(o_ref.dtype)

    return _pcall(body, name=name, grid=(T // tm,), in_specs=[_row_spec(tm, Z_W), _row_spec(tm, D), _row_spec(tm, D)],
                  out_specs=_row_spec(tm, D), out_shape=_sds((T, D), MX))(z, br_pool, br_mla)


def _merge_bwd(z, br_pool, br_mla, dmerged, *, name, tm=256):
    T = z.shape[0]

    def body(z_ref, bp_ref, bm_ref, dm_ref, dbp_ref, dbm_ref, dg_ref):
        dm = dm_ref[...].astype(F32)
        sp = _sigmoid(z_ref[:, Z_GP:Z_GP + D])
        sm = _sigmoid(z_ref[:, Z_GM:Z_GM + D])
        dbp_ref[...] = (dm * sp).astype(dbp_ref.dtype)
        dbm_ref[...] = (dm * sm).astype(dbm_ref.dtype)
        dg_ref[:, :D] = (dm * bp_ref[...].astype(F32) * sp * (1.0 - sp)).astype(dg_ref.dtype)
        dg_ref[:, D:] = (dm * bm_ref[...].astype(F32) * sm * (1.0 - sm)).astype(dg_ref.dtype)

    return _pcall(body, name=name, grid=(T // tm,),
                  in_specs=[_row_spec(tm, Z_W), _row_spec(tm, D), _row_spec(tm, D), _row_spec(tm, D)],
                  out_specs=[_row_spec(tm, D), _row_spec(tm, D), _row_spec(tm, 2 * D)],
                  out_shape=[_sds((T, D), MX), _sds((T, D), MX), _sds((T, 2 * D), MX)])(z, br_pool, br_mla, dmerged)


def _shift_down(x, k, row):
    return jnp.where(row >= k, pltpu.roll(x, k, 0), 0.0)


def _shift_up(x, k, row, n):
    return jnp.where(row < n - k, pltpu.roll(x, n - k, 0), 0.0)


def _pool_fwd(z, pool_grp, pool_scale, *, nb, name):
    T = z.shape[0]
    S = T // nb
    blk = pl.BlockSpec((S, LANE), lambda b, g: (b, g))

    def body(u_ref, w_ref, s_ref, pooled_ref, mixed_ref, scaled_ref):
        g = pl.program_id(1)
        u = u_ref[...]
        row = lax.broadcasted_iota(jnp.int32, u.shape, 0)
        s2 = u + _shift_down(u, 1, row)
        s4 = s2 + _shift_down(s2, 2, row)
        s8 = s4 + _shift_down(s4, 4, row)
        s16 = s8 + _shift_down(s8, 8, row)
        win = jnp.where(g == 0, s2, jnp.where(g == 1, s4, jnp.where(g == 2, s8, s16)))
        width = lax.shift_left(jnp.int32(2), g)
        cnt = jnp.minimum(row + 1, width).astype(F32)
        pooled = (win / cnt - u).astype(MX)
        pooled_ref[...] = pooled
        mixed = jnp.dot(pooled, w_ref[0], preferred_element_type=F32)
        mixed_ref[...] = mixed
        scaled_ref[...] = (mixed * s_ref[...]).astype(scaled_ref.dtype)

    return _pcall(body, name=name, grid=(nb, POOL_G),
                  in_specs=[blk, pl.BlockSpec((1, LANE, LANE), lambda b, g: (g, 0, 0)),
                            pl.BlockSpec((1, LANE), lambda b, g: (0, g))],
                  out_specs=[blk, blk, blk],
                  out_shape=[_sds((T, POOL_W), MX), _sds((T, POOL_W), F32), _sds((T, POOL_W), MX)])(z, pool_grp, pool_scale)


def _pool_bwd(dscaled, mixed, pooled, pool_grp, pool_scale, *, nb, name):
    T = dscaled.shape[0]
    S = T // nb
    blk = pl.BlockSpec((S, LANE), lambda g, b: (b, g))

    def body(ds_ref, mixed_ref, pooled_ref, w_ref, s_ref, du_ref, dw_ref, dsc_ref):
        g = pl.program_id(0)
        b = pl.program_id(1)
        ds = ds_ref[...]
        dsc_ref[0] = jnp.sum(ds * mixed_ref[...], axis=0, keepdims=True)
        dmixed = (ds * s_ref[...]).astype(MX)
        dw = lax.dot_general(pooled_ref[...], dmixed, (((0,), (0,)), ((), ())), preferred_element_type=F32)

        @pl.when(b == 0)
        def _():
            dw_ref[0] = dw

        @pl.when(b > 0)
        def _():
            dw_ref[0] += dw
        dpooled = lax.dot_general(dmixed, w_ref[0], (((1,), (1,)), ((), ())), preferred_element_type=F32)
        row = lax.broadcasted_iota(jnp.int32, dpooled.shape, 0)
        width = lax.shift_left(jnp.int32(2), g)
        q = dpooled / jnp.minimum(row + 1, width).astype(F32)
        r2 = q + _shift_up(q, 1, row, S)
        r4 = r2 + _shift_up(r2, 2, row, S)
        r8 = r4 + _shift_up(r4, 4, row, S)
        r16 = r8 + _shift_up(r8, 8, row, S)
        win = jnp.where(g == 0, r2, jnp.where(g == 1, r4, jnp.where(g == 2, r8, r16)))
        du_ref[...] = (win - dpooled).astype(du_ref.dtype)

    return _pcall(body, name=name, grid=(POOL_G, nb),
                  in_specs=[blk, blk, blk, pl.BlockSpec((1, LANE, LANE), lambda g, b: (g, 0, 0)),
                            pl.BlockSpec((1, LANE), lambda g, b: (0, g))],
                  out_specs=[blk, pl.BlockSpec((1, LANE, LANE), lambda g, b: (g, 0, 0)),
                             pl.BlockSpec((1, 1, LANE), lambda g, b: (b, 0, g))],
                  out_shape=[_sds((T, POOL_W), MX), _sds((POOL_G, LANE, LANE), F32), _sds((nb, 1, POOL_W), F32)],
                  )(dscaled, mixed, pooled, pool_grp, pool_scale)


def _lat_fwd(z, q_gain, kv_gain, *, name, tm=256):
    T = z.shape[0]

    def body(z_ref, qg_ref, kg_ref, qn_ref, kvn_ref):
        ql = z_ref[:, Z_QL:Z_QL + QL]
        kv = z_ref[:, Z_KV:Z_KV + KVL]
        qn_ref[...] = (ql * _rsq(ql) * qg_ref[...]).astype(qn_ref.dtype)
        kvn_ref[...] = (kv * _rsq(kv) * kg_ref[...]).astype(kvn_ref.dtype)

    return _pcall(body, name=name, grid=(T // tm,),
                  in_specs=[_row_spec(tm, LAT_W), pl.BlockSpec((1, QL), lambda i: (0, 0)), pl.BlockSpec((1, KVL), lambda i: (0, 0))],
                  out_specs=[_row_spec(tm, QL), _row_spec(tm, KVL)],
                  out_shape=[_sds((T, QL), MX), _sds((T, KVL), MX)])(z, q_gain, kv_gain)


def _lat_bwd(z, dqn, dkvn, q_gain, kv_gain, *, nb, name, tm=256):
    T = z.shape[0]
    tps = (T // nb) // tm

    def norm_bwd(x, dy, gain):
        r = _rsq(x)
        xhat = x * r
        dgain = jnp.sum(dy * xhat, axis=0, keepdims=True)
        dxhat = dy * gain
        return r * (dxhat - xhat * jnp.mean(dxhat * xhat, axis=-1, keepdims=True)), dgain

    def body(z_ref, dq_ref, dkv_ref, qg_ref, kg_ref, dql_ref, dkvl_ref, sq_ref, sk_ref):
        i = pl.program_id(0)
        dql, dqg = norm_bwd(z_ref[:, Z_QL:Z_QL + QL], dq_ref[...], qg_ref[...])
        dkvl, dkg = norm_bwd(z_ref[:, Z_KV:Z_KV + KVL], dkv_ref[...], kg_ref[...])
        dql_ref[...] = dql.astype(dql_ref.dtype)
        dkvl_ref[...] = dkvl.astype(dkvl_ref.dtype)

        @pl.when(i % tps == 0)
        def _():
            sq_ref[...] = jnp.zeros_like(sq_ref)
            sk_ref[...] = jnp.zeros_like(sk_ref)

        sq_ref[0, 0:1, :] += dqg
        sk_ref[0, 0:1, :] += dkg

    return _pcall(body, name=name, grid=(T // tm,),
                  in_specs=[_row_spec(tm, LAT_W), _row_spec(tm, QL), _row_spec(tm, KVL),
                            pl.BlockSpec((1, QL), lambda i: (0, 0)), pl.BlockSpec((1, KVL), lambda i: (0, 0))],
                  out_specs=[_row_spec(tm, QL), _row_spec(tm, KVL),
                             pl.BlockSpec((1, SUBLANE, QL), lambda i: (i // tps, 0, 0)),
                             pl.BlockSpec((1, SUBLANE, KVL), lambda i: (i // tps, 0, 0))],
                  out_shape=[_sds((T, QL), MX), _sds((T, KVL), MX), _sds((nb, SUBLANE, QL), F32), _sds((nb, SUBLANE, KVL), F32)],
                  )(z, dqn, dkvn, q_gain, kv_gain)


def _lane_masks(shape):
    lane = lax.broadcasted_iota(jnp.int32, shape, len(shape) - 1)
    m_n = lane < NOPE
    m_r = jnp.logical_and(lane >= KR_LANE, lane < KR_LANE + ROPE)
    first_half = lane < KR_LANE + ROPE // 2
    return m_n, m_r, first_half


def _rot(y, first_half):
    return jnp.where(first_half, -pltpu.roll(y, LANE - ROPE // 2, 1), pltpu.roll(y, ROPE // 2, 1))


def _rot_t(v, first_half, m_r):
    return jnp.where(m_r, jnp.where(first_half, pltpu.roll(v, LANE - ROPE // 2, 1), -pltpu.roll(v, ROPE // 2, 1)), 0.0)


def _prep_fwd(qp, kvp, z, cos, sin, q_gain, k_gain, *, name, tm=256):
    T = qp.shape[0]
    slab = pl.BlockSpec((tm, LANE), lambda i: (i, 0))
    kr_spec = pl.BlockSpec((tm, LANE), lambda i: (i, Z_KR // LANE))
    vec = pl.BlockSpec((1, LANE), lambda i: (0, 0))

    def body(qp_ref, kvp_ref, kr_ref, cos_ref, sin_ref, qg_ref, kg_ref, q_ref, k_ref, v_ref):
        m_n, m_r, first_half = _lane_masks((tm, LANE))
        c = cos_ref[...]
        s = sin_ref[...]
        qg = qg_ref[...]
        kg = kg_ref[...]
        xr = kr_ref[...]
        rr = lax.rsqrt(jnp.sum(xr * xr, axis=-1, keepdims=True) * (1.0 / ROPE) + EPS)
        yr = xr * rr * kg
        kr = jnp.where(m_r, yr * c + _rot(yr, first_half) * s, 0.0)
        for h in range(NH):
            x = qp_ref[:, h * LANE:(h + 1) * LANE]
            x2 = x * x
            rn = lax.rsqrt(jnp.sum(jnp.where(m_n, x2, 0.0), axis=-1, keepdims=True) * (1.0 / NOPE) + EPS)
            rq = lax.rsqrt(jnp.sum(jnp.where(m_r, x2, 0.0), axis=-1, keepdims=True) * (1.0 / ROPE) + EPS)
            y = x * jnp.where(m_n, rn, jnp.where(m_r, rq, 0.0)) * qg
            q_ref[:, h * LANE:(h + 1) * LANE] = (y * c + _rot(y, first_half) * s).astype(q_ref.dtype)
            xk = kvp_ref[:, h * LANE:(h + 1) * LANE]
            rk = lax.rsqrt(jnp.sum(jnp.where(m_n, xk * xk, 0.0), axis=-1, keepdims=True) * (1.0 / NOPE) + EPS)
            k_ref[:, h * LANE:(h + 1) * LANE] = (jnp.where(m_n, xk * rk * kg, 0.0) + kr).astype(k_ref.dtype)
        v_ref[...] = kvp_ref[:, NH * LANE:].astype(v_ref.dtype)

    return _pcall(body, name=name, grid=(T // tm,),
                  in_specs=[_row_spec(tm, NH * LANE), _row_spec(tm, NH * LANE + NH * NOPE), kr_spec, slab, slab, vec, vec],
                  out_specs=[_row_spec(tm, NH * LANE), _row_spec(tm, NH * LANE), _row_spec(tm, NH * NOPE)],
                  out_shape=[_sds((T, NH * LANE), MX), _sds((T, NH * LANE), MX), _sds((T, NH * NOPE), MX)],
                  )(qp, kvp, z, cos, sin, q_gain, k_gain)


def _prep_bwd(dq, dk, dv, qp, kvp, z, cos, sin, q_gain, k_gain, *, nb, name, tm=256):
    T = qp.shape[0]
    tps = (T // nb) // tm
    slab = pl.BlockSpec((tm, LANE), lambda i: (i, 0))
    kr_spec = pl.BlockSpec((tm, LANE), lambda i: (i, Z_KR // LANE))
    vec = pl.BlockSpec((1, LANE), lambda i: (0, 0))

    def body(dq_ref, dk_ref, dv_ref, qp_ref, kvp_ref, kr_ref, cos_ref, sin_ref, qg_ref, kg_ref,
             dqp_ref, dkvp_ref, dkr_ref, st_ref):
        i = pl.program_id(0)
        m_n, m_r, first_half = _lane_masks((tm, LANE))
        c = cos_ref[...]
        s = sin_ref[...]
        qg = qg_ref[...]
        kg = kg_ref[...]
        dqg = jnp.zeros((1, LANE), F32)
        dkg = jnp.zeros((1, LANE), F32)
        dkr_sum = jnp.zeros((tm, LANE), F32)
        for h in range(NH):
            x = qp_ref[:, h * LANE:(h + 1) * LANE]
            x2 = x * x
            rn = lax.rsqrt(jnp.sum(jnp.where(m_n, x2, 0.0), axis=-1, keepdims=True) * (1.0 / NOPE) + EPS)
            rq = lax.rsqrt(jnp.sum(jnp.where(m_r, x2, 0.0), axis=-1, keepdims=True) * (1.0 / ROPE) + EPS)
            rfac = jnp.where(m_n, rn, jnp.where(m_r, rq, 0.0))
            xhat = x * rfac
            do = dq_ref[:, h * LANE:(h + 1) * LANE]
            dy = do * c + _rot_t(do * s, first_half, m_r)
            dqg = dqg + jnp.sum(dy * xhat, axis=0, keepdims=True)
            dxhat = dy * qg
            t = dxhat * xhat
            mean_n = jnp.sum(jnp.where(m_n, t, 0.0), axis=-1, keepdims=True) * (1.0 / NOPE)
            mean_r = jnp.sum(jnp.where(m_r, t, 0.0), axis=-1, keepdims=True) * (1.0 / ROPE)
            dqp_ref[:, h * LANE:(h + 1) * LANE] = (
                rfac * (dxhat - xhat * jnp.where(m_n, mean_n, jnp.where(m_r, mean_r, 0.0)))).astype(dqp_ref.dtype)

            xk = kvp_ref[:, h * LANE:(h + 1) * LANE]
            rk = lax.rsqrt(jnp.sum(jnp.where(m_n, xk * xk, 0.0), axis=-1, keepdims=True) * (1.0 / NOPE) + EPS)
            khat = jnp.where(m_n, xk * rk, 0.0)
            dko = dk_ref[:, h * LANE:(h + 1) * LANE]
            dkn = jnp.where(m_n, dko, 0.0)
            dkg = dkg + jnp.sum(dkn * khat, axis=0, keepdims=True)
            dkhat = dkn * kg
            mean_k = jnp.sum(dkhat * khat, axis=-1, keepdims=True) * (1.0 / NOPE)
            dkvp_ref[:, h * LANE:(h + 1) * LANE] = jnp.where(m_n, rk * (dkhat - khat * mean_k), 0.0).astype(dkvp_ref.dtype)
            dkr_sum = dkr_sum + jnp.where(m_r, dko, 0.0)
        dkvp_ref[:, NH * LANE:] = dv_ref[...].astype(dkvp_ref.dtype)

        xr = kr_ref[...]
        rr = lax.rsqrt(jnp.sum(xr * xr, axis=-1, keepdims=True) * (1.0 / ROPE) + EPS)
        rhat = xr * rr
        dyr = dkr_sum * c + _rot_t(dkr_sum * s, first_half, m_r)
        dkg = dkg + jnp.sum(dyr * rhat, axis=0, keepdims=True)
        drhat = dyr * kg
        mean_kr = jnp.sum(drhat * rhat, axis=-1, keepdims=True) * (1.0 / ROPE)
        dkr_ref[...] = jnp.where(m_r, rr * (drhat - rhat * mean_kr), 0.0).astype(dkr_ref.dtype)

        @pl.when(i % tps == 0)
        def _():
            st_ref[...] = jnp.zeros_like(st_ref)

        st_ref[0, 0:1, :] += dqg
        st_ref[0, 1:2, :] += dkg

    return _pcall(body, name=name, grid=(T // tm,),
                  in_specs=[_row_spec(tm, NH * LANE), _row_spec(tm, NH * LANE), _row_spec(tm, NH * NOPE),
                            _row_spec(tm, NH * LANE), _row_spec(tm, NH * LANE + NH * NOPE), kr_spec, slab, slab, vec, vec],
                  out_specs=[_row_spec(tm, NH * LANE), _row_spec(tm, NH * LANE + NH * NOPE), slab,
                             pl.BlockSpec((1, SUBLANE, LANE), lambda i: (i // tps, 0, 0))],
                  out_shape=[_sds((T, NH * LANE), MX), _sds((T, NH * LANE + NH * NOPE), MX), _sds((T, LANE), MX),
                             _sds((nb, SUBLANE, LANE), F32)],
                  )(dq, dk, dv, qp, kvp, z, cos, sin, q_gain, k_gain)


def _lower_triangle(t):
    return lax.broadcasted_iota(jnp.int32, (t, t), 1) <= lax.broadcasted_iota(jnp.int32, (t, t), 0)


def _attn_fwd(q, k, v, *, nb, name, tq=512, comm=None):
    T = q.shape[0]
    S = T // nb
    tq = _pick(S, tq)
    nq = S // tq
    tk = tq
    npair = NH // 2

    def body(q_ref, k_ref, v_ref, o_ref, lse_ref):
        qi = pl.program_id(2)
        lane = lax.broadcasted_iota(jnp.int32, (tq, LANE), 1)
        qs = [q_ref[:, hh * LANE:(hh + 1) * LANE] for hh in range(2)]

        def block(j, carry, diagonal):
            k0 = pl.multiple_of(j * tk, tk)
            vb = v_ref[pl.ds(k0, tk), :]
            new = []
            for hh in range(2):
                m, l, acc = carry[hh]
                kb = k_ref[pl.ds(k0, tk), hh * LANE:(hh + 1) * LANE]
                s = lax.dot_general(qs[hh], kb, (((1,), (1,)), ((), ())), preferred_element_type=F32) * ATTN_SCALE
                if diagonal:
                    s = jnp.where(_lower_triangle(tq), s, NEG)
                m_new = jnp.maximum(m, jnp.max(s, axis=-1, keepdims=True))
                p = jnp.exp(s - m_new)
                alpha = jnp.exp(m - m_new)
                l = alpha * l + jnp.sum(p, axis=-1, keepdims=True)
                acc = alpha * acc + jnp.dot(p.astype(MX), vb, preferred_element_type=F32)
                new.append((m_new, l, acc))
            return tuple(new)

        init = tuple((jnp.full((tq, 1), NEG, F32), jnp.zeros((tq, 1), F32), jnp.zeros((tq, LANE), F32)) for _ in range(2))
        carry = lax.fori_loop(0, qi, lambda j, c: block(j, c, False), init)
        (m0, l0, acc0), (m1, l1, acc1) = block(qi, carry, True)
        o_ref[...] = jnp.where(lane < NOPE, acc0 / l0, acc1 / l1).astype(o_ref.dtype)
        lse_ref[...] = jnp.where(lane < NOPE, m0 + jnp.log(l0), m1 + jnp.log(l1))

    return _pcall(body, name=name, grid=(nb, npair, nq),
                  in_specs=[pl.BlockSpec((tq, 2 * LANE), lambda b, p, i: (b * nq + i, p)),
                            pl.BlockSpec((S, 2 * LANE), lambda b, p, i: (b, p)),
                            pl.BlockSpec((S, LANE), lambda b, p, i: (b, p))],
                  out_specs=[pl.BlockSpec((tq, LANE), lambda b, p, i: (b * nq + i, p)),
                             pl.BlockSpec((tq, LANE), lambda b, p, i: (b * nq + i, p))],
                  out_shape=[_sds((T, NH * NOPE), MX), _sds((T, NH * NOPE), F32)], comm=comm)(q, k, v)


def _attn_bwd(q, k, v, o, lse, do, *, nb, name, tq=512, comm=None):
    T = q.shape[0]
    S = T // nb
    tq = _pick(S, tq)
    nq = S // tq
    tk = tq
    npair = NH // 2

    def body(q_ref, k_ref, v_ref, o_ref, lse_ref, do_ref, dq_ref, dk_ref, dv_ref, delta_ref):
        lane = lax.broadcasted_iota(jnp.int32, (tq, LANE), 1)
        first = lane < NOPE
        dq_ref[...] = jnp.zeros_like(dq_ref)

        def delta_step(qi, _):
            q0 = pl.multiple_of(qi * tq, tq)
            prod = do_ref[pl.ds(q0, tq), :] * o_ref[pl.ds(q0, tq), :].astype(F32)
            d0 = jnp.sum(jnp.where(first, prod, 0.0), axis=-1, keepdims=True)
            d1 = jnp.sum(jnp.where(first, 0.0, prod), axis=-1, keepdims=True)
            delta_ref[pl.ds(q0, tq), :] = jnp.where(first, d0, d1)
            return 0

        lax.fori_loop(0, nq, delta_step, 0)

        def kv_step(kj, _):
            k0 = pl.multiple_of(kj * tk, tk)
            kbs = [k_ref[pl.ds(k0, tk), hh * LANE:(hh + 1) * LANE] for hh in range(2)]
            vb = v_ref[pl.ds(k0, tk), :]

            def q_block(qi, carry, diagonal):
                dk0, dk1, dv = carry
                dks = [dk0, dk1]
                q0 = pl.multiple_of(qi * tq, tq)
                dov = do_ref[pl.ds(q0, tq), :]
                lse_v = lse_ref[pl.ds(q0, tq), :]
                delta_v = delta_ref[pl.ds(q0, tq), :]
                for hh in range(2):
                    qh = q_ref[pl.ds(q0, tq), hh * LANE:(hh + 1) * LANE]
                    dob = jnp.where(first if hh == 0 else jnp.logical_not(first), dov, 0.0).astype(MX)
                    s = lax.dot_general(qh, kbs[hh], (((1,), (1,)), ((), ())), preferred_element_type=F32) * ATTN_SCALE
                    p = jnp.exp(s - lse_v[:, hh * NOPE:hh * NOPE + 1])
                    if diagonal:
                        p = jnp.where(_lower_triangle(tq), p, 0.0)
                    dp = lax.dot_general(dob, vb, (((1,), (1,)), ((), ())), preferred_element_type=F32)
                    ds = (p * (dp - delta_v[:, hh * NOPE:hh * NOPE + 1]) * ATTN_SCALE).astype(MX)
                    dks[hh] = dks[hh] + lax.dot_general(ds, qh, (((0,), (0,)), ((), ())), preferred_element_type=F32)
                    dv = dv + lax.dot_general(p.astype(MX), dob, (((0,), (0,)), ((), ())), preferred_element_type=F32)
                    dq_ref[pl.ds(q0, tq), hh * LANE:(hh + 1) * LANE] += jnp.dot(ds, kbs[hh], preferred_element_type=F32)
                return dks[0], dks[1], dv

            zero = jnp.zeros((tk, LANE), F32)
            carry = q_block(kj, (zero, zero, zero), True)
            dk0, dk1, dv = lax.fori_loop(kj + 1, nq, lambda qi, c: q_block(qi, c, False), carry)
            dk_ref[pl.ds(k0, tk), 0:LANE] = dk0
            dk_ref[pl.ds(k0, tk), LANE:2 * LANE] = dk1
            dv_ref[pl.ds(k0, tk), :] = dv
            return 0

        lax.fori_loop(0, nq, kv_step, 0)

    pair256 = pl.BlockSpec((S, 2 * LANE), lambda b, p: (b, p))
    pair128 = pl.BlockSpec((S, LANE), lambda b, p: (b, p))
    return _pcall(body, name=name, grid=(nb, npair),
                  in_specs=[pair256, pair256, pair128, pair128, pair128, pair128],
                  out_specs=[pair256, pair256, pair128],
                  out_shape=[_sds((T, NH * LANE), F32), _sds((T, NH * LANE), F32), _sds((T, NH * NOPE), F32)],
                  scratch=[pltpu.VMEM((S, LANE), F32)], comm=comm)(q, k, v, o, lse, do)


class _NoExchange:
    def __init__(self, weights):
        self.W, self.G = weights, {}

    def rider(self, name):
        return None

    def landed(self, name, outs):
        pass

    def grad(self, key, g):
        self.G[key] = g


def _run(plan, fn, *args, name, **kw):
    rider = plan.rider(name)
    if rider is None:
        return fn(*args, name=name, **kw)
    outs, landed = fn(*args, name=name, comm=rider, **kw)
    plan.landed(name, landed)
    return outs


def _layer_fwd_bwd(x, tgt, mod3, cos, sin, plan, P):
    nb = mod3.shape[0]
    W = plan.W
    h1, gu1, a1 = _run(plan, _ffn_up_first, x, mod3, P["norm_ffn1"], W["ffn1_in_lo"], sub=0, name="ffn1_up_a")
    gu1, a1 = _run(plan, _ffn_up_second, h1, W["ffn1_in_hi"], gu1, a1, name="ffn1_up_b")
    f1, x1, h2 = _run(plan, _mm_resid_norm, a1, W["ffn1_out"], x, mod3, P["norm_mix"], sub=1, coef=0.5, name="ffn1_out")
    z = _run(plan, _mm, h2, W["w_in"], mode="nn", out_dtype=F32, name="mix_in", tn=1664)
    pooled, mixed, scaled = _pool_fwd(z, W["pool_grp"], P["pool_scale"], nb=nb, name="pool_fwd")
    br_pool = _mm(scaled, W["pool_proj"], mode="nn", out_dtype=MX, name="pool_proj")
    qn, kvn = _lat_fwd(z, P["q_a_norm"], P["kv_a_norm"], name="lat_fwd")
    qp = _mm(qn, W["q_up"], mode="nn", out_dtype=F32, name="q_up")
    kvp = _mm(kvn, W["kv_up"], mode="nn", out_dtype=F32, name="kv_up")
    q, k, v = _prep_fwd(qp, kvp, z, cos, sin, P["q_gain"], P["k_gain"], name="prep_fwd")
    attn, lse = _run(plan, _attn_fwd, q, k, v, nb=nb, name="attn_fwd")
    br_mla = _mm(attn, W["mla_proj"], mode="nn", out_dtype=MX, name="mla_proj")
    merged = _merge_fwd(z, br_pool, br_mla, name="merge_fwd")
    mo, x2, h3 = _mm_resid_norm(merged, W["w_out"], x1, mod3, P["norm_ffn2"], sub=2, coef=1.0, name="mix_out")
    gu2, a2 = _ffn_up(h3, W["ffn2_in"], name="ffn2_up")
    dy, df2, st_fin, loss = _mm_loss(a2, W["ffn2_out"], x2, tgt, mod3, name="ffn2_out_loss")

    plan.grad("ffn2_out", _dw(a2, df2, name="d_ffn2_out"))
    dgu2 = _ffn_dact(df2, gu2, W["ffn2_out"], name="ffn2_dact")
    plan.grad("ffn2_in", _ffn_dw_in(h3, dgu2, name="d_ffn2_in"))
    dx2, dmo, st3 = _run(plan, _ffn_dh_norm, dgu2, W["ffn2_in"], x2, dy, mo, mod3, P["norm_ffn2"], sub=2, coef=1.0,
                         name="d_ffn2_h")
    plan.grad("w_out", _dw(merged, dmo, name="d_mix_out"))
    dmerged = _mm(dmo, W["w_out"], mode="nt", out_dtype=MX, name="d_merged")
    dbr_pool, dbr_mla, dgates = _merge_bwd(z, br_pool, br_mla, dmerged, name="merge_bwd")
    plan.grad("pool_proj", _dw(scaled, dbr_pool, name="d_pool_proj"))
    dscaled = _mm(dbr_pool, W["pool_proj"], mode="nt", out_dtype=F32, name="d_pool_scaled")
    du_pool, d_pool_grp, d_pool_scale = _pool_bwd(dscaled, mixed, pooled, W["pool_grp"], P["pool_scale"], nb=nb, name="pool_bwd")
    plan.grad("mla_proj", _dw(attn, dbr_mla, name="d_mla_proj"))
    dattn = _mm(dbr_mla, W["mla_proj"], mode="nt", out_dtype=F32, name="d_attn")
    dq, dk, dv = _run(plan, _attn_bwd, q, k, v, attn, lse, dattn, nb=nb, name="attn_bwd")
    dqp, dkvp, dkr, st_prep = _prep_bwd(dq, dk, dv, qp, kvp, z, cos, sin, P["q_gain"], P["k_gain"], nb=nb, name="prep_bwd")
    plan.grad("q_up", _dw(qn, dqp, name="d_q_up"))
    dqn = _mm(dqp, W["q_up"], mode="nt", out_dtype=F32, name="d_qn")
    plan.grad("kv_up", _dw(kvn, dkvp, name="d_kv_up"))
    dkvn = _mm(dkvp, W["kv_up"], mode="nt", out_dtype=F32, name="d_kvn")
    dql, dkvl, st_q, st_kv = _lat_bwd(z, dqn, dkvn, P["q_a_norm"], P["kv_a_norm"], nb=nb, name="lat_bwd")
    dz = jnp.concatenate([du_pool, dql, dkvl, dkr, dgates], axis=1)
    plan.grad("w_in", _run(plan, _dw, h2, dz, name="d_mix_in", tm=256, tn=1664))
    dx1, df1, st2 = _run(plan, _mm_nt_norm, dz, W["w_in"], x1, dx2, f1, mod3, P["norm_mix"], sub=1, coef=0.5, name="d_mix_h")
    plan.grad("ffn1_out", _run(plan, _dw, a1, df1, name="d_ffn1_out"))
    dgu1 = _run(plan, _ffn_dact, df1, gu1, W["ffn1_out"], name="ffn1_dact")
    half = x.shape[1] // 2
    plan.grad("ffn1_in@0", _run(plan, _ffn_dw_in, h1, dgu1, rows=(0, half), name="d_ffn1_in_a"))
    plan.grad("ffn1_in@1", _run(plan, _ffn_dw_in, h1, dgu1, rows=(half, half), name="d_ffn1_in_b"))
    dh1 = _run(plan, _ffn_dh, dgu1, W["ffn1_in_lo"], W["ffn1_in_hi"], name="d_ffn1_h")
    grad_x, st1 = _run(plan, _bwd_block, x, dh1, dx1, None, mod3, P["norm_ffn1"], sub=0, coef=0.0, name="bwd_norm1")

    return loss, grad_x, (st1, st2, st3, st_fin, st_q, st_kv, st_prep, d_pool_scale), d_pool_grp


def _w_in_to_kernel(w):
    k = w.shape[0]
    zeros = lambda n: jnp.zeros((k, n), w.dtype)
    return jnp.concatenate([w[:, 0:1152], zeros(KR_LANE), w[:, 1152:1184], zeros(LANE - KR_LANE - ROPE), w[:, 1184:]], axis=1)


def _w_in_from_kernel(g):
    return jnp.concatenate([g[:, 0:1152], g[:, Z_KR + KR_LANE:Z_KR + KR_LANE + ROPE], g[:, Z_GP:]], axis=1)


def _q_up_to_kernel(w):
    k = w.shape[0]
    return jnp.pad(w.reshape(k, NH, NOPE + ROPE), ((0, 0), (0, 0), (0, LANE - NOPE - ROPE))).reshape(k, NH * LANE)


def _q_up_from_kernel(g):
    k = g.shape[0]
    return g.reshape(k, NH, LANE)[:, :, :NOPE + ROPE].reshape(k, NH * (NOPE + ROPE))


def _kv_up_to_kernel(w):
    k = w.shape[0]
    w3 = w.reshape(k, NH, 2 * NOPE)
    kpart = jnp.pad(w3[:, :, :NOPE], ((0, 0), (0, 0), (0, LANE - NOPE))).reshape(k, NH * LANE)
    return jnp.concatenate([kpart, w3[:, :, NOPE:].reshape(k, NH * NOPE)], axis=1)


def _kv_up_from_kernel(g):
    k = g.shape[0]
    kpart = g[:, :NH * LANE].reshape(k, NH, LANE)[:, :, :NOPE]
    vpart = g[:, NH * LANE:].reshape(k, NH, NOPE)
    return jnp.concatenate([kpart, vpart], axis=2).reshape(k, NH * 2 * NOPE)


def _gain_slab(nope, rope):
    return jnp.concatenate([nope, rope, jnp.zeros((1, LANE - NOPE - ROPE), nope.dtype)], axis=1)


def _rope_tables(positions):
    inv_freq = 10000.0 ** (-jnp.arange(0, ROPE, 2, dtype=F32) / ROPE)
    ang = positions.astype(F32)[:, None] * inv_freq
    ang = jnp.concatenate([ang, ang], axis=-1)
    t = positions.shape[0]
    cos = jnp.concatenate([jnp.ones((t, KR_LANE), F32), jnp.cos(ang), jnp.ones((t, LANE - KR_LANE - ROPE), F32)], axis=1)
    sin = jnp.concatenate([jnp.zeros((t, KR_LANE), F32), jnp.sin(ang), jnp.zeros((t, LANE - KR_LANE - ROPE), F32)], axis=1)
    return cos, sin


def _coords():
    return lax.axis_index("x"), lax.axis_index("y"), lax.axis_index("c")


HBM_SPEC = pl.BlockSpec(memory_space=pl.ANY)
VMEM_SPEC = pl.BlockSpec(memory_space=pltpu.VMEM)


def _gather8(v, *, name, comm=None):
    r, c = v.shape

    def body(v_ref, out_ref, send_sems, recv_sems, local_sem):
        x, y, cc = _coords()
        me = 4 * x + 2 * y + cc
        mine = pltpu.make_async_copy(v_ref, out_ref.at[me], local_sem)
        mine.start()
        copies = []
        for kk in range(1, N_DEV):
            peer = (x ^ (kk >> 2), y ^ ((kk >> 1) & 1), cc ^ (kk & 1))
            cp = pltpu.make_async_remote_copy(src_ref=v_ref, dst_ref=out_ref.at[me], send_sem=send_sems.at[kk - 1],
                                              recv_sem=recv_sems.at[kk - 1], device_id=peer, device_id_type=MESH)
            cp.start()
            copies.append(cp)
        for kk in range(1, N_DEV):
            peer_slot = me ^ kk
            pltpu.make_async_remote_copy(src_ref=v_ref, dst_ref=out_ref.at[peer_slot], send_sem=send_sems.at[kk - 1],
                                         recv_sem=recv_sems.at[kk - 1], device_id=(x, y, cc), device_id_type=MESH).wait_recv()
        for cp in copies:
            cp.wait_send()
        mine.wait()

    return _pcall(body, name=name, out_shape=_sds((N_DEV, r, c), v.dtype), in_specs=[VMEM_SPEC], out_specs=VMEM_SPEC,
                  scratch=[pltpu.SemaphoreType.DMA((N_DEV - 1,)), pltpu.SemaphoreType.DMA((N_DEV - 1,)), pltpu.SemaphoreType.DMA],
                  comm=comm)(v)


CHIP_RELS = ((1, 0), (0, 1), (1, 1))


def _comm_only(comm, *, name):
    _, outs = _pcall(lambda: None, name=name, out_shape=[], in_specs=[], out_specs=[], comm=comm)()
    return outs


def _gather_comm(shards, only_y=None):
    n = len(shards)

    def sends(y):
        return True if only_y is None else y == only_y

    def if_(cond, fn):
        if cond is True:
            fn()
        else:
            pl.when(cond)(fn)

    def ici(ins, outs, sems, a, j, x, y, cc):
        half = ins[a].shape[0] // 2
        mine = pl.ds(cc * half, half)
        dx, dy = CHIP_RELS[j]
        return pltpu.make_async_remote_copy(src_ref=ins[a].at[mine], dst_ref=outs[a].at[2 * x + y, mine],
                                            send_sem=sems[0].at[a, j], recv_sem=sems[1].at[a, j],
                                            device_id=(x ^ dx, y ^ dy, cc), device_id_type=MESH)

    def d2d(ins, outs, sems, a, j, x, y, cc, half_of):
        half = ins[a].shape[0] // 2
        dx, dy = CHIP_RELS[j]
        landed = outs[a].at[2 * (x ^ dx) + (y ^ dy), pl.ds(half_of * half, half)]
        return pltpu.make_async_remote_copy(src_ref=landed, dst_ref=landed, send_sem=sems[2].at[a, j], recv_sem=sems[3].at[a, j],
                                            device_id=(x, y, 1 - cc), device_id_type=MESH)

    def own(ins, outs, sems, a, x, y):
        return pltpu.make_async_copy(ins[a], outs[a].at[2 * x + y], sems[4].at[a])

    def start(ins, outs, sems):
        x, y, cc = _coords()

        def go():
            for a in range(n):
                own(ins, outs, sems, a, x, y).start()
                for j in range(3):
                    ici(ins, outs, sems, a, j, x, y, cc).start()

        if_(sends(y), go)

    def finish(ins, outs, sems):
        x, y, cc = _coords()
        for j, (dx, dy) in enumerate(CHIP_RELS):
            def land_and_pass(j=j):
                for a in range(n):
                    ici(ins, outs, sems, a, j, x, y, cc).wait_recv()
                    d2d(ins, outs, sems, a, j, x, y, cc, cc).start()

            if_(sends(y ^ dy), land_and_pass)
        for j, (dx, dy) in enumerate(CHIP_RELS):
            def from_sibling(j=j):
                for a in range(n):
                    d2d(ins, outs, sems, a, j, x, y, cc, 1 - cc).wait_recv()

            if_(sends(y ^ dy), from_sibling)
        for j, (dx, dy) in enumerate(CHIP_RELS):
            def passed(j=j):
                for a in range(n):
                    d2d(ins, outs, sems, a, j, x, y, cc, cc).wait_send()

            if_(sends(y ^ dy), passed)

        def sent():
            for a in range(n):
                for j in range(3):
                    ici(ins, outs, sems, a, j, x, y, cc).wait_send()
                own(ins, outs, sems, a, x, y).wait()

        if_(sends(y), sent)

    dma = pltpu.SemaphoreType.DMA
    return _Comm(shards, [_sds((N_CHIP,) + s.shape, s.dtype) for s in shards],
                 [dma((n, 3)), dma((n, 3)), dma((n, 3)), dma((n, 3)), dma((n,))], start, finish)


def _swap_comm(parts):
    n = len(parts)

    def copy(ins, outs, sems, a):
        x, y, cc = _coords()
        half = ins[a].shape[1] // 2
        return pltpu.make_async_remote_copy(src_ref=ins[a].at[:, pl.ds((1 - cc) * half, half)], dst_ref=outs[a],
                                            send_sem=sems[0].at[a], recv_sem=sems[1].at[a], device_id=(x, y, 1 - cc),
                                            device_id_type=MESH)

    def start(ins, outs, sems):
        for a in range(n):
            copy(ins, outs, sems, a).start()

    def finish(ins, outs, sems):
        for a in range(n):
            copy(ins, outs, sems, a).wait()

    dma = pltpu.SemaphoreType.DMA
    return _Comm(parts, [_sds((p.shape[0], p.shape[1] // 2, p.shape[2]), p.dtype) for p in parts], [dma((n,)), dma((n,))],
                 start, finish)


def _add_half(full, other, cidx, *, name):
    nch, r, c = full.shape
    half = r // 2
    tr = _pick_rows(half)
    nbk = half // tr
    grid_spec = pltpu.PrefetchScalarGridSpec(
        num_scalar_prefetch=1, grid=(nch, nbk),
        in_specs=[pl.BlockSpec((1, tr, c), lambda j, i, cref: (j, cref[0] * nbk + i, 0)),
                  pl.BlockSpec((1, tr, c), lambda j, i, cref: (j, i, 0))],
        out_specs=pl.BlockSpec((1, tr, c), lambda j, i, cref: (j, i, 0)))

    def body(cref, a_ref, b_ref, o_ref):
        o_ref[...] = (a_ref[...] + b_ref[...]).astype(o_ref.dtype)

    return _pcall(body, name=name, out_shape=_sds((nch, half, c), MX), grid_spec=grid_spec)(cidx, full, other)


def _pick_rows(rows, target=512):
    best = None
    for t in range(16, min(rows, target) + 1, 16):
        if rows % t == 0:
            best = t
    return rows if best is None else best


def _exchange_comm(parts):
    n = len(parts)

    def send(ins, outs, sems, a, j, x, y, cc):
        dx, dy = CHIP_RELS[j]
        return pltpu.make_async_remote_copy(src_ref=ins[a].at[2 * (x ^ dx) + (y ^ dy)], dst_ref=outs[a].at[2 * x + y],
                                            send_sem=sems[0].at[a, j], recv_sem=sems[1].at[a, j],
                                            device_id=(x ^ dx, y ^ dy, cc), device_id_type=MESH)

    def landing(ins, outs, sems, a, j, x, y, cc):
        dx, dy = CHIP_RELS[j]
        peer_chip = 2 * (x ^ dx) + (y ^ dy)
        return pltpu.make_async_remote_copy(src_ref=ins[a].at[peer_chip], dst_ref=outs[a].at[peer_chip], send_sem=sems[0].at[a, j],
                                            recv_sem=sems[1].at[a, j], device_id=(x, y, cc), device_id_type=MESH)

    def own(ins, outs, sems, a, x, y):
        return pltpu.make_async_copy(ins[a].at[2 * x + y], outs[a].at[2 * x + y], sems[2].at[a])

    def start(ins, outs, sems):
        x, y, cc = _coords()
        for a in range(n):
            own(ins, outs, sems, a, x, y).start()
            for j in range(3):
                send(ins, outs, sems, a, j, x, y, cc).start()

    def finish(ins, outs, sems):
        x, y, cc = _coords()
        for a in range(n):
            for j in range(3):
                landing(ins, outs, sems, a, j, x, y, cc).wait_recv()
        for a in range(n):
            for j in range(3):
                send(ins, outs, sems, a, j, x, y, cc).wait_send()
            own(ins, outs, sems, a, x, y).wait()

    dma = pltpu.SemaphoreType.DMA
    return _Comm(parts, [_sds(p.shape, p.dtype) for p in parts], [dma((n, 3)), dma((n, 3)), dma((n,))], start, finish)


def _sum_chips(q, cidx, *, name):
    nch, h, c = q.shape
    tr = _pick_rows(h)
    nbk = h // tr
    grid_spec = pltpu.PrefetchScalarGridSpec(
        num_scalar_prefetch=1, grid=(nbk,),
        in_specs=[pl.BlockSpec((nch, tr, c), lambda i, cref: (0, i, 0))],
        out_specs=pl.BlockSpec((tr, c), lambda i, cref: (cref[0] * nbk + i, 0)))

    def body(cref, q_ref, o_ref):
        acc = q_ref[0].astype(F32) + q_ref[1].astype(F32)
        acc = acc + q_ref[2].astype(F32)
        o_ref[...] = acc + q_ref[3].astype(F32)

    return _pcall(body, name=name, out_shape=_sds((2 * h, c), F32), grid_spec=grid_spec)(cidx, q)


def _join_comm(fulls):
    n = len(fulls)

    def half_copy(outs, sems, a, which):
        x, y, cc = _coords()
        h = outs[a].shape[0] // 2
        rows = outs[a].at[pl.ds((cc if which == 0 else 1 - cc) * h, h)]
        return pltpu.make_async_remote_copy(src_ref=rows, dst_ref=rows, send_sem=sems[0].at[a], recv_sem=sems[1].at[a],
                                            device_id=(x, y, 1 - cc), device_id_type=MESH)

    def start(ins, outs, sems):
        for a in range(n):
            half_copy(outs, sems, a, 0).start()

    def finish(ins, outs, sems):
        for a in range(n):
            half_copy(outs, sems, a, 1).wait_recv()
        for a in range(n):
            half_copy(outs, sems, a, 0).wait_send()

    dma = pltpu.SemaphoreType.DMA
    return _Comm(fulls, [_sds(p.shape, p.dtype) for p in fulls], [dma((n,)), dma((n,))], start, finish,
                 aliases={a: a for a in range(n)})


def _ada_prologue(c, w, b, *, name, comm=None):
    nb, dm = c.shape
    n = w.shape[1]

    def body(c_ref, w_ref, b_ref, call_ref, land_ref, cpad, mod_s, g_send, g_recv, m_send, m_recv):
        x, y, cc = _coords()
        me = 4 * x + 2 * y + cc
        chip = 2 * x + y
        cpad[...] = jnp.zeros_like(cpad)
        cpad[0:nb, :] = c_ref[...]
        copies = []
        for kk in range(1, N_DEV):
            peer = (x ^ (kk >> 2), y ^ ((kk >> 1) & 1), cc ^ (kk & 1))
            cp = pltpu.make_async_remote_copy(src_ref=cpad, dst_ref=call_ref.at[me], send_sem=g_send.at[kk - 1],
                                              recv_sem=g_recv.at[kk - 1], device_id=peer, device_id_type=MESH)
            cp.start()
            copies.append(cp)
        call_ref[me] = cpad[...]
        for kk in range(1, N_DEV):
            pltpu.make_async_remote_copy(src_ref=cpad, dst_ref=call_ref.at[me ^ kk], send_sem=g_send.at[kk - 1],
                                         recv_sem=g_recv.at[kk - 1], device_id=(x, y, cc), device_id_type=MESH).wait_recv()
        cv = call_ref[...].reshape(N_DEV * SUBLANE, dm)
        act = (cv * _sigmoid(cv)).astype(MX)
        mod = jnp.dot(act, w_ref[...].astype(MX), preferred_element_type=F32) + b_ref[...]
        mod_s[...] = mod.reshape(N_DEV, SUBLANE, n)
        for j, (dx, dy) in enumerate(CHIP_RELS):
            cp = pltpu.make_async_remote_copy(src_ref=mod_s.at[4 * (x ^ dx) + 2 * (y ^ dy) + cc], dst_ref=land_ref.at[chip],
                                              send_sem=m_send.at[j], recv_sem=m_recv.at[j],
                                              device_id=(x ^ dx, y ^ dy, cc), device_id_type=MESH)
            cp.start()
            copies.append(cp)
        land_ref[chip] = mod_s[me]
        for j, (dx, dy) in enumerate(CHIP_RELS):
            pltpu.make_async_remote_copy(src_ref=mod_s.at[me], dst_ref=land_ref.at[2 * (x ^ dx) + (y ^ dy)], send_sem=m_send.at[j],
                                         recv_sem=m_recv.at[j], device_id=(x, y, cc), device_id_type=MESH).wait_recv()
        for cp in copies:
            cp.wait_send()

    dma = pltpu.SemaphoreType.DMA
    return _pcall(body, name=name, in_specs=[VMEM_SPEC] * 3, out_specs=[VMEM_SPEC] * 2,
                  out_shape=[_sds((N_DEV, SUBLANE, dm), F32), _sds((N_CHIP, SUBLANE, n), F32)],
                  scratch=[pltpu.VMEM((SUBLANE, dm), F32), pltpu.VMEM((N_DEV, SUBLANE, n), F32),
                           dma((N_DEV - 1,)), dma((N_DEV - 1,)), dma((3,)), dma((3,))], comm=comm)(c, w, b)


def _ada_bwd(c_all, dmod_cols, *, name):
    m, kdim = c_all.shape
    n = dmod_cols.shape[1]
    tn = _pick(n, 1152)

    def body(c_ref, d_ref, o_ref):
        cv = c_ref[...]
        act = (cv * _sigmoid(cv)).astype(MX)
        o_ref[...] = lax.dot_general(act, d_ref[...].astype(MX), (((0,), (0,)), ((), ())), preferred_element_type=F32)

    return _pcall(body, name=name, out_shape=_sds((kdim, n), F32), grid=(n // tn,),
                  in_specs=[pl.BlockSpec((m, kdim), lambda j: (0, 0)), pl.BlockSpec((m, tn), lambda j: (0, j))],
                  out_specs=pl.BlockSpec((kdim, tn), lambda j: (0, j)))(c_all, dmod_cols)


SLAB_W = 1024
RED_ROWS = 8


LOSS_LANE = SLAB_W - LANE


def _pack_stats(st1, st2, st3, st_fin, st_q, st_kv, st_prep, d_pool_scale, loss8, *, name):
    nb = st1.shape[0]
    dm_rows = -(-9 * nb // SUBLANE) * SUBLANE

    def body(s1, s2, s3, sf, sq, skv, sp, sps, loss_ref, o_ref):
        o_ref[...] = jnp.zeros_like(o_ref)
        for s in range(nb):
            rows = [s1[s, 0:1, :], s1[s, 1:2, :], s2[s, 3:4, :], s2[s, 0:1, :], s2[s, 1:2, :], s3[s, 3:4, :], s3[s, 0:1, :],
                    s3[s, 1:2, :], sf[s, 0:1, :]]
            for k, row in enumerate(rows):
                o_ref[9 * s + k:9 * s + k + 1, :] = row

        def over_seq(ref, r):
            acc = ref[0, r:r + 1, :]
            for s in range(1, nb):
                acc = acc + ref[s, r:r + 1, :]
            return acc

        o_ref[dm_rows + 0:dm_rows + 1, :] = over_seq(s1, 2)
        o_ref[dm_rows + 1:dm_rows + 2, :] = over_seq(s2, 2)
        o_ref[dm_rows + 2:dm_rows + 3, :] = over_seq(s3, 2)
        o_ref[dm_rows + 3:dm_rows + 4, 0:POOL_W] = over_seq(sps, 0)
        o_ref[dm_rows + 4:dm_rows + 5, 0:QL] = over_seq(sq, 0)
        o_ref[dm_rows + 5:dm_rows + 6, 0:KVL] = over_seq(skv, 0)
        o_ref[dm_rows + 6:dm_rows + 7, 0:LANE] = over_seq(sp, 0)
        o_ref[dm_rows + 7:dm_rows + 8, 0:LANE] = over_seq(sp, 1)
        o_ref[dm_rows + 7:dm_rows + 8, LOSS_LANE:] = loss_ref[0:1, :]

    return _pcall(body, name=name, out_shape=_sds((dm_rows + RED_ROWS, SLAB_W), F32), in_specs=[VMEM_SPEC] * 9,
                  out_specs=VMEM_SPEC)(st1, st2, st3, st_fin, st_q, st_kv, st_prep, d_pool_scale, loss8)


def _small_allreduce(slab, pool, *, name, comm=None):
    rows, w = slab.shape
    dm = rows - RED_ROWS
    prow, pw = pool.shape
    crow = RED_ROWS + 2 * dm

    def body(slab_ref, pool_ref, red_ref, dmod_ref, ptot_ref, sib_slab, sib_pool, chip_slab, chip_pool, land_slab, land_pool,
             a_send, a_recv, b_send, b_recv):
        x, y, cc = _coords()
        chip = 2 * x + y
        sib = (x, y, 1 - cc)
        to_sib = [pltpu.make_async_remote_copy(src_ref=slab_ref, dst_ref=sib_slab, send_sem=a_send.at[0], recv_sem=a_recv.at[0],
                                               device_id=sib, device_id_type=MESH),
                  pltpu.make_async_remote_copy(src_ref=pool_ref, dst_ref=sib_pool, send_sem=a_send.at[1], recv_sem=a_recv.at[1],
                                               device_id=sib, device_id_type=MESH)]
        for cp in to_sib:
            cp.start()
        for cp in to_sib:
            cp.wait()
        mine_dm, theirs_dm = slab_ref[0:dm, :], sib_slab[0:dm, :]
        chip_slab[0:RED_ROWS, :] = slab_ref[dm:, :] + sib_slab[dm:, :]
        chip_slab[RED_ROWS:RED_ROWS + dm, :] = jnp.where(cc == 0, mine_dm, theirs_dm)
        chip_slab[RED_ROWS + dm:, :] = jnp.where(cc == 0, theirs_dm, mine_dm)
        chip_pool[...] = pool_ref[...] + sib_pool[...]

        sends = []
        for j, (dx, dy) in enumerate(CHIP_RELS):
            peer = (x ^ dx, y ^ dy, cc)
            sends.append(pltpu.make_async_remote_copy(src_ref=chip_slab, dst_ref=land_slab.at[chip], send_sem=b_send.at[j, 0],
                                                      recv_sem=b_recv.at[j, 0], device_id=peer, device_id_type=MESH))
            sends.append(pltpu.make_async_remote_copy(src_ref=chip_pool, dst_ref=land_pool.at[chip], send_sem=b_send.at[j, 1],
                                                      recv_sem=b_recv.at[j, 1], device_id=peer, device_id_type=MESH))
        for cp in sends:
            cp.start()
        land_slab[chip] = chip_slab[...]
        land_pool[chip] = chip_pool[...]
        for j, (dx, dy) in enumerate(CHIP_RELS):
            peer_chip = 2 * (x ^ dx) + (y ^ dy)
            pltpu.make_async_remote_copy(src_ref=chip_slab, dst_ref=land_slab.at[peer_chip], send_sem=b_send.at[j, 0],
                                         recv_sem=b_recv.at[j, 0], device_id=(x, y, cc), device_id_type=MESH).wait_recv()
            pltpu.make_async_remote_copy(src_ref=chip_pool, dst_ref=land_pool.at[peer_chip], send_sem=b_send.at[j, 1],
                                         recv_sem=b_recv.at[j, 1], device_id=(x, y, cc), device_id_type=MESH).wait_recv()
        red = land_slab[0, 0:RED_ROWS, :]
        ptot = land_pool[0]
        for ch in range(1, N_CHIP):
            red = red + land_slab[ch, 0:RED_ROWS, :]
            ptot = ptot + land_pool[ch]
        red_ref[...] = red
        ptot_ref[...] = ptot
        for ch in range(N_CHIP):
            dmod_ref[2 * dm * ch:2 * dm * (ch + 1), :] = land_slab[ch, RED_ROWS:, :]
        for cp in sends:
            cp.wait_send()

    dma = pltpu.SemaphoreType.DMA
    return _pcall(body, name=name, in_specs=[VMEM_SPEC, VMEM_SPEC], out_specs=[VMEM_SPEC] * 3,
                  out_shape=[_sds((RED_ROWS, w), F32), _sds((N_DEV * dm, w), F32), _sds((prow, pw), F32)],
                  scratch=[pltpu.VMEM((rows, w), F32), pltpu.VMEM((prow, pw), F32), pltpu.VMEM((crow, w), F32),
                           pltpu.VMEM((prow, pw), F32), pltpu.VMEM((N_CHIP, crow, w), F32), pltpu.VMEM((N_CHIP, prow, pw), F32),
                           dma((2,)), dma((2,)), dma((3, 2)), dma((3, 2))], comm=comm)(slab, pool)


def _adamw_math(w, g, m, v):
    mn = ADAM_B1 * m + (1.0 - ADAM_B1) * g
    vn = ADAM_B2 * v + (1.0 - ADAM_B2) * (g * g)
    bc1 = 1.0 / (1.0 - ADAM_B1 ** ADAM_STEP)
    bc2 = 1.0 / (1.0 - ADAM_B2 ** ADAM_STEP)
    return -ADAM_LR * ((mn * bc1) / (jnp.sqrt(vn * bc2) + ADAM_EPS) + ADAM_WD * w), mn, vn


def _small_update(red, dmod_all, pool_total, nb, params, *, name):
    names = list(SMALL)
    dm = dmod_all.shape[0] // N_DEV

    def grad_of(nm, red_ref, dmod_ref, ptot_ref):
        if nm == "b_ada":
            acc = None
            for d in range(N_DEV):
                for s in range(nb):
                    blk = dmod_ref[d * dm + 9 * s:d * dm + 9 * s + 9, :]
                    acc = blk if acc is None else acc + blk
            return jnp.concatenate([acc[k:k + 1, :] for k in range(9)], axis=1)
        if nm == "pool_grp":
            return ptot_ref[...]
        row, lo, n = {"norm_ffn1": (0, 0, D), "norm_mix": (1, 0, D), "norm_ffn2": (2, 0, D), "pool_scale": (3, 0, POOL_W),
                      "q_a_norm": (4, 0, QL), "kv_a_norm": (5, 0, KVL), "q_norm_nope": (6, 0, NOPE),
                      "q_norm_rope": (6, NOPE, ROPE), "k_norm_nope": (7, 0, NOPE), "k_norm_rope": (7, KR_LANE, ROPE)}[nm]
        return red_ref[row:row + 1, lo:lo + n]

    def body(*refs):
        red_ref, dmod_ref, ptot_ref = refs[:3]
        ins = refs[3:3 + 3 * len(names)]
        outs = refs[3 + 3 * len(names):]
        outs[4 * len(names)][...] = red_ref[RED_ROWS - 1:RED_ROWS, LOSS_LANE:]
        for i, nm in enumerate(names):
            g = grad_of(nm, red_ref, dmod_ref, ptot_ref)
            d, mn, vn = _adamw_math(ins[3 * i][...], g, ins[3 * i + 1][...], ins[3 * i + 2][...])
            outs[4 * i][...] = g
            outs[4 * i + 1][...] = d
            outs[4 * i + 2][...] = mn
            outs[4 * i + 3][...] = vn

    flat_in = [a for nm in names for a in params[nm]]
    out_shape = [_sds(params[nm][0].shape, F32) for nm in names for _ in range(4)] + [_sds((1, LANE), F32)]
    res = _pcall(body, name=name, in_specs=[VMEM_SPEC] * (3 + len(flat_in)), out_specs=[VMEM_SPEC] * len(out_shape),
                 out_shape=out_shape)(red, dmod_all, pool_total, *flat_in)
    return {nm: tuple(res[4 * i:4 * i + 4]) for i, nm in enumerate(names)}, res[-1]


def _adamw(w, g, m, v, *, name, comm=None):
    r, c = w.shape
    tr = _pick_rows(r, 256)
    tc = c if tr < r else _pick(c, 256)
    spec = pl.BlockSpec((tr, tc), lambda i, j: (i, j))
    bc1 = 1.0 / (1.0 - ADAM_B1 ** ADAM_STEP)
    bc2 = 1.0 / (1.0 - ADAM_B2 ** ADAM_STEP)

    def body(w_ref, g_ref, m_ref, v_ref, d_ref, mo_ref, vo_ref):
        gv = g_ref[...]
        mn = ADAM_B1 * m_ref[...] + (1.0 - ADAM_B1) * gv
        vn = ADAM_B2 * v_ref[...] + (1.0 - ADAM_B2) * (gv * gv)
        mo_ref[...] = mn
        vo_ref[...] = vn
        d_ref[...] = -ADAM_LR * ((mn * bc1) / (jnp.sqrt(vn * bc2) + ADAM_EPS) + ADAM_WD * w_ref[...])

    out = _sds((r, c), F32)
    return _pcall(body, name=name, out_shape=[out, out, out], grid=(r // tr, c // tc), in_specs=[spec] * 4, out_specs=[spec] * 3,
                  comm=comm)(w, g, m, v)


BIG = ("w_ffn1_in", "w_ffn1_out", "w_in", "w_pool_proj", "w_q_up", "w_kv_up", "w_mla_proj", "w_out", "w_ffn2_in", "w_ffn2_out")
ROW_SHARDED = ("w_ffn1_out", "w_out", "w_ffn2_out")
KERNEL_NAME = {"w_ffn1_in": "ffn1_in", "w_ffn1_out": "ffn1_out", "w_in": "w_in", "w_pool_proj": "pool_proj", "w_q_up": "q_up",
               "w_kv_up": "kv_up", "w_mla_proj": "mla_proj", "w_out": "w_out", "w_ffn2_in": "ffn2_in", "w_ffn2_out": "ffn2_out"}
WEIGHTS = ("w_ada", "b_ada", "norm_ffn1", "w_ffn1_in", "w_ffn1_out", "norm_mix", "w_in", "pool_grp", "pool_scale", "w_pool_proj",
           "q_a_norm", "w_q_up", "kv_a_norm", "w_kv_up", "q_norm_nope", "q_norm_rope", "k_norm_nope", "k_norm_rope", "w_mla_proj",
           "w_out", "norm_ffn2", "w_ffn2_in", "w_ffn2_out")
SMALL = ("b_ada", "norm_ffn1", "norm_mix", "pool_grp", "pool_scale", "q_a_norm", "kv_a_norm", "q_norm_nope", "q_norm_rope",
         "k_norm_nope", "k_norm_rope", "norm_ffn2")
SMALL_SEQ = tuple(n for n in SMALL if n != "pool_grp")
SLAB_W = 1024


def _assemble(name, stacked):
    if name in ROW_SHARDED:
        return stacked.reshape(stacked.shape[0] * stacked.shape[1], stacked.shape[2])
    return jnp.transpose(stacked, (1, 0, 2)).reshape(stacked.shape[1], stacked.shape[0] * stacked.shape[2])


def _split(name, full):
    if name in ROW_SHARDED:
        return full.reshape(N_CHIP, full.shape[0] // N_CHIP, full.shape[1])
    return jnp.transpose(full.reshape(full.shape[0], N_CHIP, full.shape[1] // N_CHIP), (1, 0, 2))


GU = ("w_ffn1_in", "w_ffn2_in")
NARROW = ("w_in", "w_q_up")


def _to_rows(v, width=SLAB_W):
    flat = v.reshape(-1)
    rows = -(-flat.shape[0] // width)
    return jnp.pad(flat, (0, rows * width - flat.shape[0])).reshape(rows, width)


def _pack_small(parts, names=SMALL):
    rows, spans, at = [], {}, 0
    for name in names:
        r = _to_rows(parts[name])
        spans[name] = (at, parts[name].size, parts[name].shape)
        rows.append(r)
        at += r.shape[0]
    pad = (-at) % SUBLANE
    if pad:
        rows.append(jnp.zeros((pad, SLAB_W), F32))
    return jnp.concatenate(rows, axis=0), spans


def _unpack_small(slab, spans):
    out = {}
    for name, (at, size, shape) in spans.items():
        nrow = -(-size // SLAB_W)
        out[name] = slab[at:at + nrow].reshape(-1)[:size].reshape(shape)
    return out


MIX_SMALL = ("w_out", "w_pool_proj", "w_mla_proj", "w_q_up", "w_kv_up")
RIDES = {
    "ada_prologue": (("gather", ("w_ffn1_in",)),),
    "ffn1_up_a": (("gather", ("w_ffn1_out",)),),
    "ffn1_up_b": (("gather", ("w_in",)),),
    "mix_in": (("gather", MIX_SMALL),),
    "attn_fwd": (("gather", ("w_ffn2_in", "w_ffn2_out")),),
    "d_ffn2_h": (("swap", ("w_ffn2_out", "w_ffn2_in")),),
    "attn_bwd": (("exchange", ("w_ffn2_out", "w_ffn2_in")),),
    "d_mix_in": (("swap", MIX_SMALL),),
    "d_mix_h": (("exchange", MIX_SMALL), ("swap", ("w_in",))),
    "d_ffn1_out": (("exchange", ("w_in",)),),
    "ffn1_dact": (("swap", ("w_ffn1_out",)),),
    "d_ffn1_in_a": (("exchange", ("w_ffn1_out",)),),
    "d_ffn1_in_b": (("swap", ("w_ffn1_in@0",)),),
    "d_ffn1_h": (("exchange", ("w_ffn1_in@0",)), ("swap", ("w_ffn1_in@1",))),
    "bwd_norm1": (("exchange", ("w_ffn1_in@1",)),),
    "small_allreduce": (("join", tuple(n for n in BIG if n != "w_ffn1_in") + ("w_ffn1_in@0", "w_ffn1_in@1")),),
}


def _kname(n):
    base, _, part = n.partition("@")
    return KERNEL_NAME[base] + ("@" + part if part else "")


def _both(comms):
    if len(comms) == 1:
        return comms[0]
    ins, outs, sems, aliases, spans = [], [], [], {}, []
    for c in comms:
        spans.append((len(ins), len(c.ins), len(outs), len(c.out_shapes), len(sems), len(c.sems)))
        aliases.update({len(ins) + i: len(outs) + o for i, o in c.aliases.items()})
        ins, outs, sems = ins + c.ins, outs + c.out_shapes, sems + c.sems

    def each(which):
        def run(i_, o_, s_):
            for c, (ia, ni, oa, no, sa, ns) in zip(comms, spans):
                getattr(c, which)(i_[ia:ia + ni], o_[oa:oa + no], s_[sa:sa + ns])
        return run

    return _Comm(ins, outs, sems, each("start"), each("finish"), aliases)


class _ExchangePlan:
    def __init__(self, shards, cidx):
        self.shards, self.cidx = shards, cidx
        self.W, self.G, self.parts, self.pre, self.reduced, self.joined = {}, {}, {}, {}, {}, {}

    def grad(self, key, g):
        self.G[key] = g

    def rider(self, name):
        comms = []
        for kind, names in RIDES.get(name, ()):
            if kind.startswith("gather"):
                comms.append(_gather_comm([self.shards[n] for n in names], only_y={"gather_lo": 0, "gather_hi": 1}.get(kind)))
            elif kind == "swap":
                for n in names:
                    self.parts[n] = self._stacked(n)
                comms.append(_swap_comm([self.parts[n] for n in names]))
            elif kind == "exchange":
                comms.append(_exchange_comm([self.pre[n] for n in names]))
            else:
                comms.append(_join_comm([self.reduced[n] for n in names]))
        return _both(comms) if comms else None

    def landed(self, name, outs):
        at = 0
        for kind, names in RIDES[name]:
            for n, o in zip(names, outs[at:at + len(names)]):
                if kind.startswith("gather"):
                    self.W[KERNEL_NAME[n] + kind[len("gather"):]] = self._to_kernel(n, o)
                elif kind == "swap":
                    self.pre[n] = _add_half(self.parts[n], o, self.cidx, name="rs_add_" + _kname(n))
                elif kind == "exchange":
                    self.reduced[n] = _sum_chips(o, self.cidx, name="rs_sum_" + _kname(n))
                else:
                    self.joined[n] = o
            at += len(names)

    @staticmethod
    def _to_kernel(n, stacked):
        if n in GU:
            return stacked
        full = _assemble(n, stacked)
        return {"w_in": _w_in_to_kernel, "w_q_up": _q_up_to_kernel, "w_kv_up": _kv_up_to_kernel}.get(n, lambda w: w)(full)

    def _stacked(self, n):
        g = self.G[_kname(n)]
        if n.partition("@")[0] in GU:
            return g
        full = {"w_in": _w_in_from_kernel, "w_q_up": _q_up_from_kernel, "w_kv_up": _kv_up_from_kernel}.get(n, lambda w: w)(g)
        return _split(n, full)


def kernel(x, c, positions, w_ada, b_ada, norm_ffn1, w_ffn1_in, w_ffn1_out, norm_mix, w_in, pool_grp, pool_scale, w_pool_proj, q_a_norm, w_q_up, kv_a_norm, w_kv_up, q_norm_nope, q_norm_rope, k_norm_nope, k_norm_rope, w_mla_proj, w_out, norm_ffn2, w_ffn2_in, w_ffn2_out, loss_target, m_w_ada, m_b_ada, m_norm_ffn1, m_w_ffn1_in, m_w_ffn1_out, m_norm_mix, m_w_in, m_pool_grp, m_pool_scale, m_w_pool_proj, m_q_a_norm, m_w_q_up, m_kv_a_norm, m_w_kv_up, m_q_norm_nope, m_q_norm_rope, m_k_norm_nope, m_k_norm_rope, m_w_mla_proj, m_w_out, m_norm_ffn2, m_w_ffn2_in, m_w_ffn2_out, v_w_ada, v_b_ada, v_norm_ffn1, v_w_ffn1_in, v_w_ffn1_out, v_norm_mix, v_w_in, v_pool_grp, v_pool_scale, v_w_pool_proj, v_q_a_norm, v_w_q_up, v_kv_a_norm, v_w_kv_up, v_q_norm_nope, v_q_norm_rope, v_k_norm_nope, v_k_norm_rope, v_w_mla_proj, v_w_out, v_norm_ffn2, v_w_ffn2_in, v_w_ffn2_out):
    args = dict(locals())
    wts = {n: args[n][0] for n in WEIGHTS}
    mom = {n: args["m_" + n][0] for n in WEIGHTS}
    var = {n: args["v_" + n][0] for n in WEIGHTS}
    nb, seq, dm = x.shape
    tokens = nb * seq
    xi, yi, ci = _coords()
    chip = 2 * xi + yi
    dev = 4 * xi + 2 * yi + ci

    cidx = ci.astype(jnp.int32).reshape(1)
    plan = _ExchangePlan({n: wts[n].astype(MX) for n in BIG}, cidx)
    plan.W["pool_grp"] = wts["pool_grp"].astype(MX)

    ncol = w_ada.shape[2]
    b_cols = lax.dynamic_slice_in_dim(wts["b_ada"].reshape(1, -1), chip * ncol, ncol, axis=1)
    c_slots, mod_slots = _run(plan, _ada_prologue, c, wts["w_ada"], b_cols, name="ada_prologue")
    plan.W["ffn1_in_lo"] = plan.W["ffn1_in_hi"] = plan.W["ffn1_in"]
    c_all = c_slots[:, :nb].reshape(N_DEV * nb, dm)
    mod3 = jnp.transpose(mod_slots[:, :nb], (1, 0, 2)).reshape(nb, 9, dm)
    P = {"norm_ffn1": wts["norm_ffn1"].reshape(1, dm), "norm_mix": wts["norm_mix"].reshape(1, dm),
         "norm_ffn2": wts["norm_ffn2"].reshape(1, dm), "pool_scale": wts["pool_scale"].reshape(1, POOL_W),
         "q_a_norm": wts["q_a_norm"].reshape(1, QL), "kv_a_norm": wts["kv_a_norm"].reshape(1, KVL),
         "q_gain": _gain_slab(wts["q_norm_nope"].reshape(1, NOPE), wts["q_norm_rope"].reshape(1, ROPE)),
         "k_gain": _gain_slab(wts["k_norm_nope"].reshape(1, NOPE), wts["k_norm_rope"].reshape(1, ROPE))}
    cos, sin = _rope_tables(positions.reshape(tokens))

    loss8, grad_x, stats, d_pool_grp = _layer_fwd_bwd(x.reshape(tokens, dm), loss_target.reshape(tokens, dm), mod3, cos, sin, plan, P)

    slab = _pack_stats(*stats, loss8, name="pack_stats")
    red, dmod_rows, pool_total = _run(plan, _small_allreduce, slab, d_pool_grp.reshape(POOL_G * LANE, LANE), name="small_allreduce")
    grads = {n: plan.joined[n] for n in BIG if n != "w_ffn1_in"}
    grads["w_ffn1_in"] = jnp.concatenate([plan.joined["w_ffn1_in@0"], plan.joined["w_ffn1_in@1"]], axis=0)
    dm_rows = dmod_rows.shape[0] // N_DEV
    dmod_all = dmod_rows.reshape(N_DEV, dm_rows, SLAB_W)[:, :9 * nb].reshape(N_DEV * nb, 9 * dm)
    dmod_cols = lax.dynamic_slice_in_dim(dmod_all, chip * ncol, ncol, axis=1)
    grads["w_ada"] = _ada_bwd(c_all, dmod_cols, name="ada_bwd")

    delta, new_m, new_v = {}, {}, {}
    as2d = lambda a: a.reshape(POOL_G * LANE, LANE) if a.ndim == 4 else a.reshape(1, -1)
    upd, loss_row = _small_update(red, dmod_rows, pool_total, nb,
                                  {n: tuple(as2d(args[p + n]) for p in ("", "m_", "v_")) for n in SMALL}, name="small_update")
    loss = loss_row[0, 0]
    for n in SMALL:
        grads[n], delta[n], new_m[n], new_v[n] = upd[n]
    for n in ("w_ada",) + BIG:
        if n in NARROW:
            res = _adamw(wts[n].T, grads[n].T, mom[n].T, var[n].T, name="adamw_" + n)
            delta[n], new_m[n], new_v[n] = (r.T for r in res)
        else:
            delta[n], new_m[n], new_v[n] = _adamw(wts[n], grads[n], mom[n], var[n], name="adamw_" + n)

    def lead(a, n):
        return a.reshape((1,) + wts[n].shape)

    return (loss, grad_x.reshape(nb, seq, dm), *[lead(grads[n], n) for n in WEIGHTS], *[lead(delta[n], n) for n in WEIGHTS],
            *[lead(new_m[n], n) for n in WEIGHTS], *[lead(new_v[n], n) for n in WEIGHTS])
```

```python
import functools
import math

import jax
import jax.numpy as jnp
from jax import lax
from jax.experimental import pallas as pl
from jax.experimental.pallas import tpu as pltpu

F32 = jnp.float32
MX = jnp.bfloat16

D = 1024
DFF = 2816
NH = 8
POOL_W = 512
POOL_G = 4
QL = 384
KVL = 256
ROPE = 32
NOPE = 64
LANE = 128
SUBLANE = 8
EPS = 1e-6
ATTN_SCALE = 1.0 / math.sqrt(96.0)
NEG = -1e30

Z_UP, Z_QL, Z_KV, Z_KR, Z_GP, Z_GM, Z_W = 0, 512, 896, 1152, 1280, 2304, 3328
KR_LANE = 64
LAT_W = 1280

ADAM_LR, ADAM_B1, ADAM_B2, ADAM_EPS, ADAM_WD, ADAM_STEP = 0.001, 0.9, 0.999, 1e-08, 0.01, 10

VMEM_LIMIT = 48 * 1024 * 1024
MESH = pl.DeviceIdType.MESH
N_DEV = 8
N_CHIP = 4


class _Comm:
    def __init__(self, ins, out_shapes, sems, start, finish, aliases=None):
        self.ins, self.out_shapes, self.sems = list(ins), list(out_shapes), list(sems)
        self.start, self.finish = start, finish
        self.aliases = aliases or {}


def _pcall(body, *, name, out_shape, grid=(), in_specs=None, out_specs=None, scratch=(), grid_spec=None, aliases=None,
           comm=None):
    params = pltpu.CompilerParams(vmem_limit_bytes=VMEM_LIMIT)
    kw = dict(name=name, compiler_params=params)
    if comm is None:
        if aliases:
            kw["input_output_aliases"] = aliases
        if grid_spec is not None:
            return pl.pallas_call(body, grid_spec=grid_spec, out_shape=out_shape, **kw)
        return pl.pallas_call(body, grid=grid, in_specs=in_specs, out_specs=out_specs, scratch_shapes=scratch,
                              out_shape=out_shape, **kw)
    single = not isinstance(out_shape, (list, tuple))
    outs = [out_shape] if single else list(out_shape)
    ospecs = [out_specs] if single else list(out_specs)
    n_in, n_out, n_ci, n_co, n_scr = len(in_specs), len(outs), len(comm.ins), len(comm.out_shapes), len(scratch)
    io = dict(aliases or {})
    io.update({n_in + i: n_out + o for i, o in comm.aliases.items()})

    def riding(*refs):
        ins, cins = refs[:n_in], refs[n_in:n_in + n_ci]
        at = n_in + n_ci
        os_, couts = refs[at:at + n_out], refs[at + n_out:at + n_out + n_co]
        at += n_out + n_co
        scr, csems = refs[at:at + n_scr], refs[at + n_scr:]
        if grid:
            first = functools.reduce(jnp.logical_and, [pl.program_id(d) == 0 for d in range(len(grid))])
            last = functools.reduce(jnp.logical_and, [pl.program_id(d) == grid[d] - 1 for d in range(len(grid))])
            pl.when(first)(lambda: comm.start(cins, couts, csems))
            body(*ins, *os_, *scr)
            pl.when(last)(lambda: comm.finish(cins, couts, csems))
        else:
            comm.start(cins, couts, csems)
            body(*ins, *os_, *scr)
            comm.finish(cins, couts, csems)

    call = pl.pallas_call(riding, grid=grid, in_specs=list(in_specs) + [HBM_SPEC] * n_ci, out_specs=ospecs + [HBM_SPEC] * n_co,
                          out_shape=outs + comm.out_shapes, scratch_shapes=list(scratch) + comm.sems,
                          input_output_aliases=io, **kw)

    def run(*args):
        res = call(*args, *comm.ins)
        main = list(res[:n_out])
        return (main[0] if single else main), list(res[n_out:])

    return run


def _pick(dim, target):
    best = None
    for t in range(LANE, min(dim, target) + 1, LANE):
        if dim % t == 0:
            best = t
    return dim if best is None else best


def _sds(shape, dtype):
    return jax.ShapeDtypeStruct(shape, dtype)


def _dw(a, g, *, name, tm=512, tn=1024, comm=None):
    return _mm(a, g, mode="tn", out_dtype=F32, name=name, tm=tm, tn=tn, tk=a.shape[0], n_outer=True, comm=comm)


def _mm(a, b, *, mode, out_dtype, name, tm=1024, tn=1024, tk=4096, n_outer=False, comm=None):
    if mode == "nn":
        (M, K), (K2, N) = a.shape, b.shape
    elif mode == "nt":
        (M, K), (N, K2) = a.shape, b.shape
    else:
        (K, M), (K2, N) = a.shape, b.shape
    assert K == K2, (name, a.shape, b.shape)
    tm, tn, tk = _pick(M, tm), _pick(N, tn), _pick(K, tk)
    nk = K // tk
    if n_outer:
        ij = lambda g0, g1: (g1, g0)
        grid = (N // tn, M // tm, nk)
    else:
        ij = lambda g0, g1: (g0, g1)
        grid = (M // tm, N // tn, nk)
    if mode == "tn":
        a_spec = pl.BlockSpec((tk, tm), lambda g0, g1, k: (k, ij(g0, g1)[0]))
    else:
        a_spec = pl.BlockSpec((tm, tk), lambda g0, g1, k: (ij(g0, g1)[0], k))
    if mode == "nt":
        b_spec = pl.BlockSpec((tn, tk), lambda g0, g1, k: (ij(g0, g1)[1], k))
    else:
        b_spec = pl.BlockSpec((tk, tn), lambda g0, g1, k: (k, ij(g0, g1)[1]))
    o_spec = pl.BlockSpec((tm, tn), lambda g0, g1, k: ij(g0, g1))
    dn = {"nn": (((1,), (0,)), ((), ())), "nt": (((1,), (1,)), ((), ())), "tn": (((0,), (0,)), ((), ()))}[mode]

    def dot(a_ref, b_ref):
        return lax.dot_general(a_ref[...].astype(MX), b_ref[...].astype(MX), dn, preferred_element_type=F32)

    def body_one(a_ref, b_ref, o_ref):
        o_ref[...] = dot(a_ref, b_ref).astype(o_ref.dtype)

    def body_acc(a_ref, b_ref, o_ref, acc_ref):
        k = pl.program_id(2)
        part = dot(a_ref, b_ref)

        @pl.when(k == 0)
        def _():
            acc_ref[...] = part

        @pl.when(k > 0)
        def _():
            acc_ref[...] += part

        @pl.when(k == nk - 1)
        def _():
            o_ref[...] = acc_ref[...].astype(o_ref.dtype)

    return _pcall(body_one if nk == 1 else body_acc, name=name, out_shape=_sds((M, N), out_dtype), grid=grid,
                  in_specs=[a_spec, b_spec], out_specs=o_spec, scratch=[] if nk == 1 else [pltpu.VMEM((tm, tn), F32)],
                  comm=comm)(a, b)


def _gu_shard(q):
    return (q % 2) * 2 + q // 2


def _ffn_up(h, w_st, *, name, tm=512, comm=None):
    T, dm = h.shape
    hw = w_st.shape[2]
    tm = _pick(T, tm)

    def body(h_ref, wg_ref, wu_ref, gu_ref, a_ref):
        hv = h_ref[...]
        g = jnp.dot(hv, wg_ref[0], preferred_element_type=F32)
        u = jnp.dot(hv, wu_ref[0], preferred_element_type=F32)
        gu_ref[:, :hw] = g.astype(gu_ref.dtype)
        gu_ref[:, hw:] = u.astype(gu_ref.dtype)
        a_ref[...] = (g * _sigmoid(g) * u).astype(a_ref.dtype)

    return _pcall(body, name=name, grid=(T // tm, 2),
                  in_specs=[pl.BlockSpec((tm, dm), lambda i, j: (i, 0)), pl.BlockSpec((1, dm, hw), lambda i, j: (j, 0, 0)),
                            pl.BlockSpec((1, dm, hw), lambda i, j: (2 + j, 0, 0))],
                  out_specs=[pl.BlockSpec((tm, 2 * hw), lambda i, j: (i, j)), pl.BlockSpec((tm, hw), lambda i, j: (i, j))],
                  out_shape=[_sds((T, 4 * hw), MX), _sds((T, 2 * hw), MX)], comm=comm)(h, w_st, w_st)


def _ffn_up_first(x, mod3, gain, w_st, *, sub, name, tm=512, comm=None):
    T, dm = x.shape
    hw = w_st.shape[2]
    tm = _pick(T, tm)
    tps = (T // mod3.shape[0]) // tm

    def body(x_ref, mod_ref, n_ref, wg_ref, wu_ref, h_ref, gu_ref, a_ref):
        xv = x_ref[...]
        xn = xv * _rsq(xv) * n_ref[...]
        hv = (xn * (1.0 + mod_ref[0, 3 * sub + 1:3 * sub + 2, :]) + mod_ref[0, 3 * sub:3 * sub + 1, :]).astype(h_ref.dtype)
        h_ref[...] = hv
        g = jnp.dot(hv, wg_ref[0], preferred_element_type=F32)
        u = jnp.dot(hv, wu_ref[0], preferred_element_type=F32)
        gu_ref[:, :hw] = g.astype(gu_ref.dtype)
        gu_ref[:, hw:] = u.astype(gu_ref.dtype)
        a_ref[...] = (g * _sigmoid(g) * u).astype(a_ref.dtype)

    return _pcall(body, name=name, grid=(T // tm,),
                  in_specs=[_row_spec(tm, dm), pl.BlockSpec((1, 9, dm), lambda i: (i // tps, 0, 0)),
                            pl.BlockSpec((1, dm), lambda i: (0, 0)), pl.BlockSpec((1, dm, hw), lambda i: (0, 0, 0)),
                            pl.BlockSpec((1, dm, hw), lambda i: (2, 0, 0))],
                  out_specs=[_row_spec(tm, dm), pl.BlockSpec((tm, 2 * hw), lambda i: (i, 0)), pl.BlockSpec((tm, hw), lambda i: (i, 0))],
                  out_shape=[_sds((T, dm), MX), _sds((T, 4 * hw), MX), _sds((T, 2 * hw), MX)],
                  comm=comm)(x, mod3, gain, w_st, w_st)


def _ffn_up_second(h, w_st, gu, a, *, name, tm=512, comm=None):
    T, dm = h.shape
    hw = w_st.shape[2]
    tm = _pick(T, tm)

    def body(h_ref, wg_ref, wu_ref, gu_in, a_in, gu_ref, a_ref):
        hv = h_ref[...]
        g = jnp.dot(hv, wg_ref[0], preferred_element_type=F32)
        u = jnp.dot(hv, wu_ref[0], preferred_element_type=F32)
        gu_ref[:, :hw] = g.astype(gu_ref.dtype)
        gu_ref[:, hw:] = u.astype(gu_ref.dtype)
        a_ref[...] = (g * _sigmoid(g) * u).astype(a_ref.dtype)

    return _pcall(body, name=name, grid=(T // tm,),
                  in_specs=[_row_spec(tm, dm), pl.BlockSpec((1, dm, hw), lambda i: (1, 0, 0)),
                            pl.BlockSpec((1, dm, hw), lambda i: (3, 0, 0)), HBM_SPEC, HBM_SPEC],
                  out_specs=[pl.BlockSpec((tm, 2 * hw), lambda i: (i, 1)), pl.BlockSpec((tm, hw), lambda i: (i, 1))],
                  out_shape=[_sds(gu.shape, gu.dtype), _sds(a.shape, a.dtype)], aliases={3: 0, 4: 1},
                  comm=comm)(h, w_st, w_st, gu, a)


def _ffn_dact(df, gu, w_out, *, name, tm=512, comm=None):
    T, dm = df.shape
    hw = gu.shape[1] // 4
    tm = _pick(T, tm)

    def body(df_ref, gu_ref, wo_ref, dgu_ref):
        da = lax.dot_general(df_ref[...], wo_ref[...], (((1,), (1,)), ((), ())), preferred_element_type=F32)
        g = gu_ref[:, :hw].astype(F32)
        u = gu_ref[:, hw:].astype(F32)
        s = _sigmoid(g)
        dgu_ref[:, :hw] = (da * u * (s * (1.0 + g * (1.0 - s)))).astype(dgu_ref.dtype)
        dgu_ref[:, hw:] = (da * (g * s)).astype(dgu_ref.dtype)

    return _pcall(body, name=name, grid=(T // tm, 2),
                  in_specs=[pl.BlockSpec((tm, dm), lambda i, j: (i, 0)), pl.BlockSpec((tm, 2 * hw), lambda i, j: (i, j)),
                            pl.BlockSpec((hw, dm), lambda i, j: (j, 0))],
                  out_specs=pl.BlockSpec((tm, 2 * hw), lambda i, j: (i, j)), out_shape=_sds(gu.shape, MX),
                  comm=comm)(df, gu, w_out)


def _ffn_dh(dgu, w_st, *, name, tm=1024, comm=None):
    T = dgu.shape[0]
    _, dm, hw = w_st.shape
    tm = _pick(T, tm)

    def body(d_ref, w_ref, o_ref, acc_ref):
        q = pl.program_id(1)
        part = lax.dot_general(d_ref[...], w_ref[0], (((1,), (1,)), ((), ())), preferred_element_type=F32)

        @pl.when(q == 0)
        def _():
            acc_ref[...] = part

        @pl.when(jnp.logical_and(q > 0, q < 3))
        def _():
            acc_ref[...] += part

        @pl.when(q == 3)
        def _():
            o_ref[...] = acc_ref[...] + part

    return _pcall(body, name=name, grid=(T // tm, 4),
                  in_specs=[pl.BlockSpec((tm, hw), lambda i, q: (i, q)), pl.BlockSpec((1, dm, hw), lambda i, q: (_gu_shard(q), 0, 0))],
                  out_specs=pl.BlockSpec((tm, dm), lambda i, q: (i, 0)), out_shape=_sds((T, dm), F32),
                  scratch=[pltpu.VMEM((tm, dm), F32)], comm=comm)(dgu, w_st)


def _ffn_dw_in(h, dgu, *, name, rows=None, tm=512, comm=None):
    T, dm = h.shape
    hw = dgu.shape[1] // 4
    first, count = rows if rows is not None else (0, dm)
    tm = _pick(count, tm)
    skip = first // tm

    def body(h_ref, d_ref, o_ref):
        o_ref[0] = lax.dot_general(h_ref[...], d_ref[...], (((0,), (0,)), ((), ())), preferred_element_type=F32)

    return _pcall(body, name=name, grid=(4, count // tm),
                  in_specs=[pl.BlockSpec((T, tm), lambda q, i: (0, skip + i)), pl.BlockSpec((T, hw), lambda q, i: (0, q))],
                  out_specs=pl.BlockSpec((1, tm, hw), lambda q, i: (_gu_shard(q), i, 0)),
                  out_shape=_sds((4, count, hw), F32), comm=comm)(h, dgu)


def _rsq(x):
    return lax.rsqrt(jnp.mean(x * x, axis=-1, keepdims=True) + EPS)


def _sigmoid(x):
    return 1.0 / (1.0 + jnp.exp(-x))


def _row_spec(tm, w):
    return pl.BlockSpec((tm, w), lambda i: (i, 0))


def _mm_resid_norm(a, w, x_prev, mod3, gain, *, sub, coef, name, tm=512, comm=None):
    T, k = a.shape
    dm = w.shape[1]
    tm = _pick(T, tm)
    tps = (T // mod3.shape[0]) // tm

    def body(a_ref, w_ref, x_ref, mod_ref, n_ref, f_ref, xo_ref, h_ref):
        f = jnp.dot(a_ref[...], w_ref[...], preferred_element_type=F32)
        f_ref[...] = f
        x = x_ref[...] + coef * mod_ref[0, 3 * sub - 1:3 * sub, :] * f
        xo_ref[...] = x
        xn = x * _rsq(x) * n_ref[...]
        h_ref[...] = (xn * (1.0 + mod_ref[0, 3 * sub + 1:3 * sub + 2, :]) + mod_ref[0, 3 * sub:3 * sub + 1, :]).astype(h_ref.dtype)

    row = _row_spec(tm, dm)
    return _pcall(body, name=name, grid=(T // tm,),
                  in_specs=[_row_spec(tm, k), pl.BlockSpec((k, dm), lambda i: (0, 0)), row,
                            pl.BlockSpec((1, 9, dm), lambda i: (i // tps, 0, 0)), pl.BlockSpec((1, dm), lambda i: (0, 0))],
                  out_specs=[row, row, row], out_shape=[_sds((T, dm), F32), _sds((T, dm), F32), _sds((T, dm), MX)],
                  comm=comm)(a, w, x_prev, mod3, gain)


def _mm_loss(a, w, x2, tgt, mod3, *, name, tm=512):
    T, k = a.shape
    dm = w.shape[1]
    tm = _pick(T, tm)
    tps = (T // mod3.shape[0]) // tm
    mod_spec = pl.BlockSpec((1, 9, dm), lambda i: (i // tps, 0, 0))
    row = _row_spec(tm, dm)
    stat_spec = pl.BlockSpec((1, SUBLANE, dm), lambda i: (i // tps, 0, 0))
    loss_spec = pl.BlockSpec((SUBLANE, LANE), lambda i: (0, 0))

    def body(a_ref, w_ref, x_ref, t_ref, mod_ref, dy_ref, df_ref, st_ref, loss_ref):
        i = pl.program_id(0)
        g = mod_ref[0, 8:9, :]
        f = jnp.dot(a_ref[...], w_ref[...], preferred_element_type=F32)
        err = x_ref[...] + 0.5 * g * f - t_ref[...]
        dy = err * (1.0 / dm)
        dy_ref[...] = dy
        df_ref[...] = (0.5 * g * dy).astype(df_ref.dtype)
        dgate = jnp.sum(0.5 * dy * f, axis=0, keepdims=True)
        part = 0.5 * jnp.sum(jnp.sum(err * err, axis=0, keepdims=True), axis=1, keepdims=True) * (1.0 / dm)

        @pl.when(i % tps == 0)
        def _():
            st_ref[...] = jnp.zeros_like(st_ref)

        @pl.when(i == 0)
        def _():
            loss_ref[...] = jnp.zeros_like(loss_ref)

        st_ref[0, 0:1, :] += dgate
        loss_ref[...] += jnp.broadcast_to(part, loss_ref.shape)

    return _pcall(body, name=name, grid=(T // tm,),
                  in_specs=[_row_spec(tm, k), pl.BlockSpec((k, dm), lambda i: (0, 0)), row, row, mod_spec],
                  out_specs=[row, row, stat_spec, loss_spec],
                  out_shape=[_sds((T, dm), F32), _sds((T, dm), MX), _sds((mod3.shape[0], SUBLANE, dm), F32),
                             _sds((SUBLANE, LANE), F32)])(a, w, x2, tgt, mod3)


def _norm_bwd_tail(dhv, x_ref, dxi_ref, f_ref, mod_ref, n_ref, dx_ref, df_ref, st_ref, *, first, sub, coef):
    x = x_ref[...]
    r = _rsq(x)
    xhat = x * r
    n = n_ref[...]
    d_shift = jnp.sum(dhv, axis=0, keepdims=True)
    d_scale = jnp.sum(dhv * (xhat * n), axis=0, keepdims=True)
    dxn = dhv * (1.0 + mod_ref[0, 3 * sub + 1:3 * sub + 2, :])
    d_gain = jnp.sum(dxn * xhat, axis=0, keepdims=True)
    dxhat = dxn * n
    dx = dxi_ref[...] + r * (dxhat - xhat * jnp.mean(dxhat * xhat, axis=-1, keepdims=True))
    dx_ref[...] = dx

    @pl.when(first)
    def _():
        st_ref[...] = jnp.zeros_like(st_ref)

    st_ref[0, 0:1, :] += d_shift
    st_ref[0, 1:2, :] += d_scale
    st_ref[0, 2:3, :] += d_gain
    if f_ref is not None:
        st_ref[0, 3:4, :] += jnp.sum(coef * dx * f_ref[...], axis=0, keepdims=True)
        df_ref[...] = (coef * mod_ref[0, 3 * sub - 1:3 * sub, :] * dx).astype(df_ref.dtype)


def _ffn_dh_norm(dgu, w_st, x_cur, dx_in, f_prev, mod3, gain, *, sub, coef, name, tm=512, comm=None):
    T = dgu.shape[0]
    _, dm, hw = w_st.shape
    tm = _pick(T, tm)
    nb = mod3.shape[0]
    tps = (T // nb) // tm

    def body(d_ref, w_ref, x_ref, dxi_ref, f_ref, mod_ref, n_ref, dx_ref, df_ref, st_ref, acc_ref):
        i = pl.program_id(0)
        q = pl.program_id(1)
        part = lax.dot_general(d_ref[...], w_ref[0], (((1,), (1,)), ((), ())), preferred_element_type=F32)

        @pl.when(q == 0)
        def _():
            acc_ref[...] = part

        @pl.when(q > 0)
        def _():
            acc_ref[...] += part

        @pl.when(q == 3)
        def _():
            _norm_bwd_tail(acc_ref[...], x_ref, dxi_ref, f_ref, mod_ref, n_ref, dx_ref, df_ref, st_ref,
                           first=i % tps == 0, sub=sub, coef=coef)

    row = pl.BlockSpec((tm, dm), lambda i, q: (i, 0))
    return _pcall(body, name=name, grid=(T // tm, 4),
                  in_specs=[pl.BlockSpec((tm, hw), lambda i, q: (i, q)), pl.BlockSpec((1, dm, hw), lambda i, q: (_gu_shard(q), 0, 0)),
                            row, row, row, pl.BlockSpec((1, 9, dm), lambda i, q: (i // tps, 0, 0)),
                            pl.BlockSpec((1, dm), lambda i, q: (0, 0))],
                  out_specs=[row, row, pl.BlockSpec((1, SUBLANE, dm), lambda i, q: (i // tps, 0, 0))],
                  out_shape=[_sds((T, dm), F32), _sds((T, dm), MX), _sds((nb, SUBLANE, dm), F32)],
                  scratch=[pltpu.VMEM((tm, dm), F32)], comm=comm)(dgu, w_st, x_cur, dx_in, f_prev, mod3, gain)


def _mm_nt_norm(a, b, x_cur, dx_in, f_prev, mod3, gain, *, sub, coef, name, tm=512, comm=None):
    T, k = a.shape
    dm = b.shape[0]
    tm = _pick(T, tm)
    nb = mod3.shape[0]
    tps = (T // nb) // tm

    def body(a_ref, b_ref, x_ref, dxi_ref, f_ref, mod_ref, n_ref, dx_ref, df_ref, st_ref):
        dh = lax.dot_general(a_ref[...], b_ref[...], (((1,), (1,)), ((), ())), preferred_element_type=F32)
        _norm_bwd_tail(dh, x_ref, dxi_ref, f_ref, mod_ref, n_ref, dx_ref, df_ref, st_ref,
                       first=pl.program_id(0) % tps == 0, sub=sub, coef=coef)

    row = _row_spec(tm, dm)
    return _pcall(body, name=name, grid=(T // tm,),
                  in_specs=[_row_spec(tm, k), pl.BlockSpec((dm, k), lambda i: (0, 0)), row, row, row,
                            pl.BlockSpec((1, 9, dm), lambda i: (i // tps, 0, 0)), pl.BlockSpec((1, dm), lambda i: (0, 0))],
                  out_specs=[row, row, pl.BlockSpec((1, SUBLANE, dm), lambda i: (i // tps, 0, 0))],
                  out_shape=[_sds((T, dm), F32), _sds((T, dm), MX), _sds((nb, SUBLANE, dm), F32)],
                  comm=comm)(a, b, x_cur, dx_in, f_prev, mod3, gain)


def _bwd_block(x_cur, dh, dx_in, f_prev, mod3, gain, *, sub, coef, name, tm=256, comm=None):
    T, dm = x_cur.shape
    nb = mod3.shape[0]
    tps = (T // nb) // tm
    has_f = f_prev is not None
    mod_spec = pl.BlockSpec((1, 9, dm), lambda i: (i // tps, 0, 0))
    vec_spec = pl.BlockSpec((1, dm), lambda i: (0, 0))
    stat_spec = pl.BlockSpec((1, SUBLANE, dm), lambda i: (i // tps, 0, 0))
    row = _row_spec(tm, dm)

    def body(*refs):
        if has_f:
            x_ref, dh_ref, dxi_ref, f_ref, mod_ref, n_ref, dx_ref, df_ref, st_ref = refs
        else:
            x_ref, dh_ref, dxi_ref, mod_ref, n_ref, dx_ref, st_ref = refs
            f_ref = df_ref = None
        _norm_bwd_tail(dh_ref[...], x_ref, dxi_ref, f_ref, mod_ref, n_ref, dx_ref, df_ref, st_ref,
                       first=pl.program_id(0) % tps == 0, sub=sub, coef=coef)

    st_shape = _sds((nb, SUBLANE, dm), F32)
    if has_f:
        return _pcall(body, name=name, grid=(T // tm,), in_specs=[row, row, row, row, mod_spec, vec_spec],
                      out_specs=[row, row, stat_spec],
                      out_shape=[_sds((T, dm), F32), _sds((T, dm), MX), st_shape], comm=comm)(x_cur, dh, dx_in, f_prev, mod3, gain)
    return _pcall(body, name=name, grid=(T // tm,), in_specs=[row, row, row, mod_spec, vec_spec],
                  out_specs=[row, stat_spec], out_shape=[_sds((T, dm), F32), st_shape], comm=comm)(x_cur, dh, dx_in, mod3, gain)


def _merge_fwd(z, br_pool, br_mla, *, name, tm=256):
    T = z.shape[0]

    def body(z_ref, bp_ref, bm_ref, o_ref):
        gp = z_ref[:, Z_GP:Z_GP + D]
        gm = z_ref[:, Z_GM:Z_GM + D]
        o_ref[...] = (_sigmoid(gp) * bp_ref[...] + _sigmoid(gm) * bm_ref[...]).astype(o_ref.dtype)

    return _pcall(body, name=name, grid=(T // tm,), in_specs=[_row_spec(tm, Z_W), _row_spec(tm, D), _row_spec(tm, D)],
                  out_specs=_row_spec(tm, D), out_shape=_sds((T, D), MX))(z, br_pool, br_mla)


def _merge_bwd(z, br_pool, br_mla, dmerged, *, name, tm=256):
    T = z.shape[0]

    def body(z_ref, bp_ref, bm_ref, dm_ref, dbp_ref, dbm_ref, dg_ref):
        dm = dm_ref[...].astype(F32)
        sp = _sigmoid(z_ref[:, Z_GP:Z_GP + D])
        sm = _sigmoid(z_ref[:, Z_GM:Z_GM + D])
        dbp_ref[...] = (dm * sp).astype(dbp_ref.dtype)
        dbm_ref[...] = (dm * sm).astype(dbm_ref.dtype)
        dg_ref[:, :D] = (dm * bp_ref[...].astype(F32) * sp * (1.0 - sp)).astype(dg_ref.dtype)
        dg_ref[:, D:] = (dm * bm_ref[...].astype(F32) * sm * (1.0 - sm)).astype(dg_ref.dtype)

    return _pcall(body, name=name, grid=(T // tm,),
                  in_specs=[_row_spec(tm, Z_W), _row_spec(tm, D), _row_spec(tm, D), _row_spec(tm, D)],
                  out_specs=[_row_spec(tm, D), _row_spec(tm, D), _row_spec(tm, 2 * D)],
                  out_shape=[_sds((T, D), MX), _sds((T, D), MX), _sds((T, 2 * D), MX)])(z, br_pool, br_mla, dmerged)


def _shift_down(x, k, row):
    return jnp.where(row >= k, pltpu.roll(x, k, 0), 0.0)


def _shift_up(x, k, row, n):
    return jnp.where(row < n - k, pltpu.roll(x, n - k, 0), 0.0)


def _pool_fwd(z, pool_grp, pool_scale, *, nb, name):
    T = z.shape[0]
    S = T // nb
    blk = pl.BlockSpec((S, LANE), lambda b, g: (b, g))

    def body(u_ref, w_ref, s_ref, pooled_ref, mixed_ref, scaled_ref):
        g = pl.program_id(1)
        u = u_ref[...]
        row = lax.broadcasted_iota(jnp.int32, u.shape, 0)
        s2 = u + _shift_down(u, 1, row)
        s4 = s2 + _shift_down(s2, 2, row)
        s8 = s4 + _shift_down(s4, 4, row)
        s16 = s8 + _shift_down(s8, 8, row)
        win = jnp.where(g == 0, s2, jnp.where(g == 1, s4, jnp.where(g == 2, s8, s16)))
        width = lax.shift_left(jnp.int32(2), g)
        cnt = jnp.minimum(row + 1, width).astype(F32)
        pooled = (win / cnt - u).astype(MX)
        pooled_ref[...] = pooled
        mixed = jnp.dot(pooled, w_ref[0], preferred_element_type=F32)
        mixed_ref[...] = mixed
        scaled_ref[...] = (mixed * s_ref[...]).astype(scaled_ref.dtype)

    return _pcall(body, name=name, grid=(nb, POOL_G),
                  in_specs=[blk, pl.BlockSpec((1, LANE, LANE), lambda b, g: (g, 0, 0)),
                            pl.BlockSpec((1, LANE), lambda b, g: (0, g))],
                  out_specs=[blk, blk, blk],
                  out_shape=[_sds((T, POOL_W), MX), _sds((T, POOL_W), F32), _sds((T, POOL_W), MX)])(z, pool_grp, pool_scale)


def _pool_bwd(dscaled, mixed, pooled, pool_grp, pool_scale, *, nb, name):
    T = dscaled.shape[0]
    S = T // nb
    blk = pl.BlockSpec((S, LANE), lambda g, b: (b, g))

    def body(ds_ref, mixed_ref, pooled_ref, w_ref, s_ref, du_ref, dw_ref, dsc_ref):
        g = pl.program_id(0)
        b = pl.program_id(1)
        ds = ds_ref[...]
        dsc_ref[0] = jnp.sum(ds * mixed_ref[...], axis=0, keepdims=True)
        dmixed = (ds * s_ref[...]).astype(MX)
        dw = lax.dot_general(pooled_ref[...], dmixed, (((0,), (0,)), ((), ())), preferred_element_type=F32)

        @pl.when(b == 0)
        def _():
            dw_ref[0] = dw

        @pl.when(b > 0)
        def _():
            dw_ref[0] += dw
        dpooled = lax.dot_general(dmixed, w_ref[0], (((1,), (1,)), ((), ())), preferred_element_type=F32)
        row = lax.broadcasted_iota(jnp.int32, dpooled.shape, 0)
        width = lax.shift_left(jnp.int32(2), g)
        q = dpooled / jnp.minimum(row + 1, width).astype(F32)
        r2 = q + _shift_up(q, 1, row, S)
        r4 = r2 + _shift_up(r2, 2, row, S)
        r8 = r4 + _shift_up(r4, 4, row, S)
        r16 = r8 + _shift_up(r8, 8, row, S)
        win = jnp.where(g == 0, r2, jnp.where(g == 1, r4, jnp.where(g == 2, r8, r16)))
        du_ref[...] = (win - dpooled).astype(du_ref.dtype)

    return _pcall(body, name=name, grid=(POOL_G, nb),
                  in_specs=[blk, blk, blk, pl.BlockSpec((1, LANE, LANE), lambda g, b: (g, 0, 0)),
                            pl.BlockSpec((1, LANE), lambda g, b: (0, g))],
                  out_specs=[blk, pl.BlockSpec((1, LANE, LANE), lambda g, b: (g, 0, 0)),
                             pl.BlockSpec((1, 1, LANE), lambda g, b: (b, 0, g))],
                  out_shape=[_sds((T, POOL_W), MX), _sds((POOL_G, LANE, LANE), F32), _sds((nb, 1, POOL_W), F32)],
                  )(dscaled, mixed, pooled, pool_grp, pool_scale)


def _lat_fwd(z, q_gain, kv_gain, *, name, tm=256):
    T = z.shape[0]

    def body(z_ref, qg_ref, kg_ref, qn_ref, kvn_ref):
        ql = z_ref[:, Z_QL:Z_QL + QL]
        kv = z_ref[:, Z_KV:Z_KV + KVL]
        qn_ref[...] = (ql * _rsq(ql) * qg_ref[...]).astype(qn_ref.dtype)
        kvn_ref[...] = (kv * _rsq(kv) * kg_ref[...]).astype(kvn_ref.dtype)

    return _pcall(body, name=name, grid=(T // tm,),
                  in_specs=[_row_spec(tm, LAT_W), pl.BlockSpec((1, QL), lambda i: (0, 0)), pl.BlockSpec((1, KVL), lambda i: (0, 0))],
                  out_specs=[_row_spec(tm, QL), _row_spec(tm, KVL)],
                  out_shape=[_sds((T, QL), MX), _sds((T, KVL), MX)])(z, q_gain, kv_gain)


def _lat_bwd(z, dqn, dkvn, q_gain, kv_gain, *, nb, name, tm=256):
    T = z.shape[0]
    tps = (T // nb) // tm

    def norm_bwd(x, dy, gain):
        r = _rsq(x)
        xhat = x * r
        dgain = jnp.sum(dy * xhat, axis=0, keepdims=True)
        dxhat = dy * gain
        return r * (dxhat - xhat * jnp.mean(dxhat * xhat, axis=-1, keepdims=True)), dgain

    def body(z_ref, dq_ref, dkv_ref, qg_ref, kg_ref, dql_ref, dkvl_ref, sq_ref, sk_ref):
        i = pl.program_id(0)
        dql, dqg = norm_bwd(z_ref[:, Z_QL:Z_QL + QL], dq_ref[...], qg_ref[...])
        dkvl, dkg = norm_bwd(z_ref[:, Z_KV:Z_KV + KVL], dkv_ref[...], kg_ref[...])
        dql_ref[...] = dql.astype(dql_ref.dtype)
        dkvl_ref[...] = dkvl.astype(dkvl_ref.dtype)

        @pl.when(i % tps == 0)
        def _():
            sq_ref[...] = jnp.zeros_like(sq_ref)
            sk_ref[...] = jnp.zeros_like(sk_ref)

        sq_ref[0, 0:1, :] += dqg
        sk_ref[0, 0:1, :] += dkg

    return _pcall(body, name=name, grid=(T // tm,),
                  in_specs=[_row_spec(tm, LAT_W), _row_spec(tm, QL), _row_spec(tm, KVL),
                            pl.BlockSpec((1, QL), lambda i: (0, 0)), pl.BlockSpec((1, KVL), lambda i: (0, 0))],
                  out_specs=[_row_spec(tm, QL), _row_spec(tm, KVL),
                             pl.BlockSpec((1, SUBLANE, QL), lambda i: (i // tps, 0, 0)),
                             pl.BlockSpec((1, SUBLANE, KVL), lambda i: (i // tps, 0, 0))],
                  out_shape=[_sds((T, QL), MX), _sds((T, KVL), MX), _sds((nb, SUBLANE, QL), F32), _sds((nb, SUBLANE, KVL), F32)],
                  )(z, dqn, dkvn, q_gain, kv_gain)


def _lane_masks(shape):
    lane = lax.broadcasted_iota(jnp.int32, shape, len(shape) - 1)
    m_n = lane < NOPE
    m_r = jnp.logical_and(lane >= KR_LANE, lane < KR_LANE + ROPE)
    first_half = lane < KR_LANE + ROPE // 2
    return m_n, m_r, first_half


def _rot(y, first_half):
    return jnp.where(first_half, -pltpu.roll(y, LANE - ROPE // 2, 1), pltpu.roll(y, ROPE // 2, 1))


def _rot_t(v, first_half, m_r):
    return jnp.where(m_r, jnp.where(first_half, pltpu.roll(v, LANE - ROPE // 2, 1), -pltpu.roll(v, ROPE // 2, 1)), 0.0)


def _prep_fwd(qp, kvp, z, cos, sin, q_gain, k_gain, *, name, tm=256):
    T = qp.shape[0]
    slab = pl.BlockSpec((tm, LANE), lambda i: (i, 0))
    kr_spec = pl.BlockSpec((tm, LANE), lambda i: (i, Z_KR // LANE))
    vec = pl.BlockSpec((1, LANE), lambda i: (0, 0))

    def body(qp_ref, kvp_ref, kr_ref, cos_ref, sin_ref, qg_ref, kg_ref, q_ref, k_ref, v_ref):
        m_n, m_r, first_half = _lane_masks((tm, LANE))
        c = cos_ref[...]
        s = sin_ref[...]
        qg = qg_ref[...]
        kg = kg_ref[...]
        xr = kr_ref[...]
        rr = lax.rsqrt(jnp.sum(xr * xr, axis=-1, keepdims=True) * (1.0 / ROPE) + EPS)
        yr = xr * rr * kg
        kr = jnp.where(m_r, yr * c + _rot(yr, first_half) * s, 0.0)
        for h in range(NH):
            x = qp_ref[:, h * LANE:(h + 1) * LANE]
            x2 = x * x
            rn = lax.rsqrt(jnp.sum(jnp.where(m_n, x2, 0.0), axis=-1, keepdims=True) * (1.0 / NOPE) + EPS)
            rq = lax.rsqrt(jnp.sum(jnp.where(m_r, x2, 0.0), axis=-1, keepdims=True) * (1.0 / ROPE) + EPS)
            y = x * jnp.where(m_n, rn, jnp.where(m_r, rq, 0.0)) * qg
            q_ref[:, h * LANE:(h + 1) * LANE] = (y * c + _rot(y, first_half) * s).astype(q_ref.dtype)
            xk = kvp_ref[:, h * LANE:(h + 1) * LANE]
            rk = lax.rsqrt(jnp.sum(jnp.where(m_n, xk * xk, 0.0), axis=-1, keepdims=True) * (1.0 / NOPE) + EPS)
            k_ref[:, h * LANE:(h + 1) * LANE] = (jnp.where(m_n, xk * rk * kg, 0.0) + kr).astype(k_ref.dtype)
        v_ref[...] = kvp_ref[:, NH * LANE:].astype(v_ref.dtype)

    return _pcall(body, name=name, grid=(T // tm,),
                  in_specs=[_row_spec(tm, NH * LANE), _row_spec(tm, NH * LANE + NH * NOPE), kr_spec, slab, slab, vec, vec],
                  out_specs=[_row_spec(tm, NH * LANE), _row_spec(tm, NH * LANE), _row_spec(tm, NH * NOPE)],
                  out_shape=[_sds((T, NH * LANE), MX), _sds((T, NH * LANE), MX), _sds((T, NH * NOPE), MX)],
                  )(qp, kvp, z, cos, sin, q_gain, k_gain)


def _prep_bwd(dq, dk, dv, qp, kvp, z, cos, sin, q_gain, k_gain, *, nb, name, tm=256):
    T = qp.shape[0]
    tps = (T // nb) // tm
    slab = pl.BlockSpec((tm, LANE), lambda i: (i, 0))
    kr_spec = pl.BlockSpec((tm, LANE), lambda i: (i, Z_KR // LANE))
    vec = pl.BlockSpec((1, LANE), lambda i: (0, 0))

    def body(dq_ref, dk_ref, dv_ref, qp_ref, kvp_ref, kr_ref, cos_ref, sin_ref, qg_ref, kg_ref,
             dqp_ref, dkvp_ref, dkr_ref, st_ref):
        i = pl.program_id(0)
        m_n, m_r, first_half = _lane_masks((tm, LANE))
        c = cos_ref[...]
        s = sin_ref[...]
        qg = qg_ref[...]
        kg = kg_ref[...]
        dqg = jnp.zeros((1, LANE), F32)
        dkg = jnp.zeros((1, LANE), F32)
        dkr_sum = jnp.zeros((tm, LANE), F32)
        for h in range(NH):
            x = qp_ref[:, h * LANE:(h + 1) * LANE]
            x2 = x * x
            rn = lax.rsqrt(jnp.sum(jnp.where(m_n, x2, 0.0), axis=-1, keepdims=True) * (1.0 / NOPE) + EPS)
            rq = lax.rsqrt(jnp.sum(jnp.where(m_r, x2, 0.0), axis=-1, keepdims=True) * (1.0 / ROPE) + EPS)
            rfac = jnp.where(m_n, rn, jnp.where(m_r, rq, 0.0))
            xhat = x * rfac
            do = dq_ref[:, h * LANE:(h + 1) * LANE]
            dy = do * c + _rot_t(do * s, first_half, m_r)
            dqg = dqg + jnp.sum(dy * xhat, axis=0, keepdims=True)
            dxhat = dy * qg
            t = dxhat * xhat
            mean_n = jnp.sum(jnp.where(m_n, t, 0.0), axis=-1, keepdims=True) * (1.0 / NOPE)
            mean_r = jnp.sum(jnp.where(m_r, t, 0.0), axis=-1, keepdims=True) * (1.0 / ROPE)
            dqp_ref[:, h * LANE:(h + 1) * LANE] = (
                rfac * (dxhat - xhat * jnp.where(m_n, mean_n, jnp.where(m_r, mean_r, 0.0)))).astype(dqp_ref.dtype)

            xk = kvp_ref[:, h * LANE:(h + 1) * LANE]
            rk = lax.rsqrt(jnp.sum(jnp.where(m_n, xk * xk, 0.0), axis=-1, keepdims=True) * (1.0 / NOPE) + EPS)
            khat = jnp.where(m_n, xk * rk, 0.0)
            dko = dk_ref[:, h * LANE:(h + 1) * LANE]
            dkn = jnp.where(m_n, dko, 0.0)
            dkg = dkg + jnp.sum(dkn * khat, axis=0, keepdims=True)
            dkhat = dkn * kg
            mean_k = jnp.sum(dkhat * khat, axis=-1, keepdims=True) * (1.0 / NOPE)
            dkvp_ref[:, h * LANE:(h + 1) * LANE] = jnp.where(m_n, rk * (dkhat - khat * mean_k), 0.0).astype(dkvp_ref.dtype)
            dkr_sum = dkr_sum + jnp.where(m_r, dko, 0.0)
        dkvp_ref[:, NH * LANE:] = dv_ref[...].astype(dkvp_ref.dtype)

        xr = kr_ref[...]
        rr = lax.rsqrt(jnp.sum(xr * xr, axis=-1, keepdims=True) * (1.0 / ROPE) + EPS)
        rhat = xr * rr
        dyr = dkr_sum * c + _rot_t(dkr_sum * s, first_half, m_r)
        dkg = dkg + jnp.sum(dyr * rhat, axis=0, keepdims=True)
        drhat = dyr * kg
        mean_kr = jnp.sum(drhat * rhat, axis=-1, keepdims=True) * (1.0 / ROPE)
        dkr_ref[...] = jnp.where(m_r, rr * (drhat - rhat * mean_kr), 0.0).astype(dkr_ref.dtype)

        @pl.when(i % tps == 0)
        def _():
            st_ref[...] = jnp.zeros_like(st_ref)

        st_ref[0, 0:1, :] += dqg
        st_ref[0, 1:2, :] += dkg

    return _pcall(body, name=name, grid=(T // tm,),
                  in_specs=[_row_spec(tm, NH * LANE), _row_spec(tm, NH * LANE), _row_spec(tm, NH * NOPE),
                            _row_spec(tm, NH * LANE), _row_spec(tm, NH * LANE + NH * NOPE), kr_spec, slab, slab, vec, vec],
                  out_specs=[_row_spec(tm, NH * LANE), _row_spec(tm, NH * LANE + NH * NOPE), slab,
                             pl.BlockSpec((1, SUBLANE, LANE), lambda i: (i // tps, 0, 0))],
                  out_shape=[_sds((T, NH * LANE), MX), _sds((T, NH * LANE + NH * NOPE), MX), _sds((T, LANE), MX),
                             _sds((nb, SUBLANE, LANE), F32)],
                  )(dq, dk, dv, qp, kvp, z, cos, sin, q_gain, k_gain)


def _lower_triangle(t):
    return lax.broadcasted_iota(jnp.int32, (t, t), 1) <= lax.broadcasted_iota(jnp.int32, (t, t), 0)


def _attn_fwd(q, k, v, *, nb, name, tq=512, comm=None):
    T = q.shape[0]
    S = T // nb
    tq = _pick(S, tq)
    nq = S // tq
    tk = tq
    npair = NH // 2

    def body(q_ref, k_ref, v_ref, o_ref, lse_ref):
        qi = pl.program_id(2)
        lane = lax.broadcasted_iota(jnp.int32, (tq, LANE), 1)
        qs = [q_ref[:, hh * LANE:(hh + 1) * LANE] for hh in range(2)]

        def block(j, carry, diagonal):
            k0 = pl.multiple_of(j * tk, tk)
            vb = v_ref[pl.ds(k0, tk), :]
            new = []
            for hh in range(2):
                m, l, acc = carry[hh]
                kb = k_ref[pl.ds(k0, tk), hh * LANE:(hh + 1) * LANE]
                s = lax.dot_general(qs[hh], kb, (((1,), (1,)), ((), ())), preferred_element_type=F32) * ATTN_SCALE
                if diagonal:
                    s = jnp.where(_lower_triangle(tq), s, NEG)
                m_new = jnp.maximum(m, jnp.max(s, axis=-1, keepdims=True))
                p = jnp.exp(s - m_new)
                alpha = jnp.exp(m - m_new)
                l = alpha * l + jnp.sum(p, axis=-1, keepdims=True)
                acc = alpha * acc + jnp.dot(p.astype(MX), vb, preferred_element_type=F32)
                new.append((m_new, l, acc))
            return tuple(new)

        init = tuple((jnp.full((tq, 1), NEG, F32), jnp.zeros((tq, 1), F32), jnp.zeros((tq, LANE), F32)) for _ in range(2))
        carry = lax.fori_loop(0, qi, lambda j, c: block(j, c, False), init)
        (m0, l0, acc0), (m1, l1, acc1) = block(qi, carry, True)
        o_ref[...] = jnp.where(lane < NOPE, acc0 / l0, acc1 / l1).astype(o_ref.dtype)
        lse_ref[...] = jnp.where(lane < NOPE, m0 + jnp.log(l0), m1 + jnp.log(l1))

    return _pcall(body, name=name, grid=(nb, npair, nq),
                  in_specs=[pl.BlockSpec((tq, 2 * LANE), lambda b, p, i: (b * nq + i, p)),
                            pl.BlockSpec((S, 2 * LANE), lambda b, p, i: (b, p)),
                            pl.BlockSpec((S, LANE), lambda b, p, i: (b, p))],
                  out_specs=[pl.BlockSpec((tq, LANE), lambda b, p, i: (b * nq + i, p)),
                             pl.BlockSpec((tq, LANE), lambda b, p, i: (b * nq + i, p))],
                  out_shape=[_sds((T, NH * NOPE), MX), _sds((T, NH * NOPE), F32)], comm=comm)(q, k, v)


def _attn_bwd(q, k, v, o, lse, do, *, nb, name, tq=512, comm=None):
    T = q.shape[0]
    S = T // nb
    tq = _pick(S, tq)
    nq = S // tq
    tk = tq
    npair = NH // 2

    def body(q_ref, k_ref, v_ref, o_ref, lse_ref, do_ref, dq_ref, dk_ref, dv_ref, delta_ref):
        lane = lax.broadcasted_iota(jnp.int32, (tq, LANE), 1)
        first = lane < NOPE
        dq_ref[...] = jnp.zeros_like(dq_ref)

        def delta_step(qi, _):
            q0 = pl.multiple_of(qi * tq, tq)
            prod = do_ref[pl.ds(q0, tq), :] * o_ref[pl.ds(q0, tq), :].astype(F32)
            d0 = jnp.sum(jnp.where(first, prod, 0.0), axis=-1, keepdims=True)
            d1 = jnp.sum(jnp.where(first, 0.0, prod), axis=-1, keepdims=True)
            delta_ref[pl.ds(q0, tq), :] = jnp.where(first, d0, d1)
            return 0

        lax.fori_loop(0, nq, delta_step, 0)

        def kv_step(kj, _):
            k0 = pl.multiple_of(kj * tk, tk)
            kbs = [k_ref[pl.ds(k0, tk), hh * LANE:(hh + 1) * LANE] for hh in range(2)]
            vb = v_ref[pl.ds(k0, tk), :]

            def q_block(qi, carry, diagonal):
                dk0, dk1, dv = carry
                dks = [dk0, dk1]
                q0 = pl.multiple_of(qi * tq, tq)
                dov = do_ref[pl.ds(q0, tq), :]
                lse_v = lse_ref[pl.ds(q0, tq), :]
                delta_v = delta_ref[pl.ds(q0, tq), :]
                for hh in range(2):
                    qh = q_ref[pl.ds(q0, tq), hh * LANE:(hh + 1) * LANE]
                    dob = jnp.where(first if hh == 0 else jnp.logical_not(first), dov, 0.0).astype(MX)
                    s = lax.dot_general(qh, kbs[hh], (((1,), (1,)), ((), ())), preferred_element_type=F32) * ATTN_SCALE
                    p = jnp.exp(s - lse_v[:, hh * NOPE:hh * NOPE + 1])
                    if diagonal:
                        p = jnp.where(_lower_triangle(tq), p, 0.0)
                    dp = lax.dot_general(dob, vb, (((1,), (1,)), ((), ())), preferred_element_type=F32)
                    ds = (p * (dp - delta_v[:, hh * NOPE:hh * NOPE + 1]) * ATTN_SCALE).astype(MX)
                    dks[hh] = dks[hh] + lax.dot_general(ds, qh, (((0,), (0,)), ((), ())), preferred_element_type=F32)
                    dv = dv + lax.dot_general(p.astype(MX), dob, (((0,), (0,)), ((), ())), preferred_element_type=F32)
                    dq_ref[pl.ds(q0, tq), hh * LANE:(hh + 1) * LANE] += jnp.dot(ds, kbs[hh], preferred_element_type=F32)
                return dks[0], dks[1], dv

            zero = jnp.zeros((tk, LANE), F32)
            carry = q_block(kj, (zero, zero, zero), True)
            dk0, dk1, dv = lax.fori_loop(kj + 1, nq, lambda qi, c: q_block(qi, c, False), carry)
            dk_ref[pl.ds(k0, tk), 0:LANE] = dk0
            dk_ref[pl.ds(k0, tk), LANE:2 * LANE] = dk1
            dv_ref[pl.ds(k0, tk), :] = dv
            return 0

        lax.fori_loop(0, nq, kv_step, 0)

    pair256 = pl.BlockSpec((S, 2 * LANE), lambda b, p: (b, p))
    pair128 = pl.BlockSpec((S, LANE), lambda b, p: (b, p))
    return _pcall(body, name=name, grid=(nb, npair),
                  in_specs=[pair256, pair256, pair128, pair128, pair128, pair128],
                  out_specs=[pair256, pair256, pair128],
                  out_shape=[_sds((T, NH * LANE), F32), _sds((T, NH * LANE), F32), _sds((T, NH * NOPE), F32)],
                  scratch=[pltpu.VMEM((S, LANE), F32)], comm=comm)(q, k, v, o, lse, do)


def _run(plan, fn, *args, name, **kw):
    rider = plan.rider(name)
    if rider is None:
        return fn(*args, name=name, **kw)
    outs, landed = fn(*args, name=name, comm=rider, **kw)
    plan.landed(name, landed)
    return outs


def _layer_fwd_bwd(x, tgt, mod3, cos, sin, plan, P):
    nb = mod3.shape[0]
    W = plan.W
    h1, gu1, a1 = _run(plan, _ffn_up_first, x, mod3, P["norm_ffn1"], W["ffn1_in"], sub=0, name="ffn1_up_a")
    gu1, a1 = _run(plan, _ffn_up_second, h1, W["ffn1_in"], gu1, a1, name="ffn1_up_b")
    f1, x1, h2 = _run(plan, _mm_resid_norm, a1, W["ffn1_out"], x, mod3, P["norm_mix"], sub=1, coef=0.5, name="ffn1_out")
    z = _run(plan, _mm, h2, W["w_in"], mode="nn", out_dtype=F32, name="mix_in", tn=1664)
    pooled, mixed, scaled = _pool_fwd(z, W["pool_grp"], P["pool_scale"], nb=nb, name="pool_fwd")
    br_pool = _mm(scaled, W["pool_proj"], mode="nn", out_dtype=MX, name="pool_proj")
    qn, kvn = _lat_fwd(z, P["q_a_norm"], P["kv_a_norm"], name="lat_fwd")
    qp = _mm(qn, W["q_up"], mode="nn", out_dtype=F32, name="q_up")
    kvp = _mm(kvn, W["kv_up"], mode="nn", out_dtype=F32, name="kv_up")
    q, k, v = _prep_fwd(qp, kvp, z, cos, sin, P["q_gain"], P["k_gain"], name="prep_fwd")
    attn, lse = _run(plan, _attn_fwd, q, k, v, nb=nb, name="attn_fwd")
    br_mla = _mm(attn, W["mla_proj"], mode="nn", out_dtype=MX, name="mla_proj")
    merged = _merge_fwd(z, br_pool, br_mla, name="merge_fwd")
    mo, x2, h3 = _mm_resid_norm(merged, W["w_out"], x1, mod3, P["norm_ffn2"], sub=2, coef=1.0, name="mix_out")
    gu2, a2 = _ffn_up(h3, W["ffn2_in"], name="ffn2_up")
    dy, df2, st_fin, loss = _mm_loss(a2, W["ffn2_out"], x2, tgt, mod3, name="ffn2_out_loss")

    plan.grad("ffn2_out", _dw(a2, df2, name="d_ffn2_out"))
    dgu2 = _ffn_dact(df2, gu2, W["ffn2_out"], name="ffn2_dact")
    plan.grad("ffn2_in", _ffn_dw_in(h3, dgu2, name="d_ffn2_in"))
    dx2, dmo, st3 = _run(plan, _ffn_dh_norm, dgu2, W["ffn2_in"], x2, dy, mo, mod3, P["norm_ffn2"], sub=2, coef=1.0,
                         name="d_ffn2_h")
    plan.grad("w_out", _dw(merged, dmo, name="d_mix_out"))
    dmerged = _mm(dmo, W["w_out"], mode="nt", out_dtype=MX, name="d_merged")
    dbr_pool, dbr_mla, dgates = _merge_bwd(z, br_pool, br_mla, dmerged, name="merge_bwd")
    plan.grad("pool_proj", _dw(scaled, dbr_pool, name="d_pool_proj"))
    dscaled = _mm(dbr_pool, W["pool_proj"], mode="nt", out_dtype=F32, name="d_pool_scaled")
    du_pool, d_pool_grp, d_pool_scale = _pool_bwd(dscaled, mixed, pooled, W["pool_grp"], P["pool_scale"], nb=nb, name="pool_bwd")
    plan.grad("mla_proj", _dw(attn, dbr_mla, name="d_mla_proj"))
    dattn = _mm(dbr_mla, W["mla_proj"], mode="nt", out_dtype=F32, name="d_attn")
    dq, dk, dv = _run(plan, _attn_bwd, q, k, v, attn, lse, dattn, nb=nb, name="attn_bwd")
    dqp, dkvp, dkr, st_prep = _prep_bwd(dq, dk, dv, qp, kvp, z, cos, sin, P["q_gain"], P["k_gain"], nb=nb, name="prep_bwd")
    plan.grad("q_up", _dw(qn, dqp, name="d_q_up"))
    dqn = _mm(dqp, W["q_up"], mode="nt", out_dtype=F32, name="d_qn")
    plan.grad("kv_up", _dw(kvn, dkvp, name="d_kv_up"))
    dkvn = _mm(dkvp, W["kv_up"], mode="nt", out_dtype=F32, name="d_kvn")
    dql, dkvl, st_q, st_kv = _lat_bwd(z, dqn, dkvn, P["q_a_norm"], P["kv_a_norm"], nb=nb, name="lat_bwd")
    dz = jnp.concatenate([du_pool, dql, dkvl, dkr, dgates], axis=1)
    plan.grad("w_in", _run(plan, _dw, h2, dz, name="d_mix_in", tm=256, tn=1664))
    dx1, df1, st2 = _run(plan, _mm_nt_norm, dz, W["w_in"], x1, dx2, f1, mod3, P["norm_mix"], sub=1, coef=0.5, name="d_mix_h")
    plan.grad("ffn1_out", _run(plan, _dw, a1, df1, name="d_ffn1_out"))
    dgu1 = _run(plan, _ffn_dact, df1, gu1, W["ffn1_out"], name="ffn1_dact")
    half = x.shape[1] // 2
    plan.grad("ffn1_in@0", _run(plan, _ffn_dw_in, h1, dgu1, rows=(0, half), name="d_ffn1_in_a"))
    plan.grad("ffn1_in@1", _run(plan, _ffn_dw_in, h1, dgu1, rows=(half, half), name="d_ffn1_in_b"))
    dh1 = _run(plan, _ffn_dh, dgu1, W["ffn1_in"], name="d_ffn1_h")
    grad_x, st1 = _run(plan, _bwd_block, x, dh1, dx1, None, mod3, P["norm_ffn1"], sub=0, coef=0.0, name="bwd_norm1")

    return loss, grad_x, (st1, st2, st3, st_fin, st_q, st_kv, st_prep, d_pool_scale), d_pool_grp


def _w_in_to_kernel(w):
    k = w.shape[0]
    zeros = lambda n: jnp.zeros((k, n), w.dtype)
    return jnp.concatenate([w[:, 0:1152], zeros(KR_LANE), w[:, 1152:1184], zeros(LANE - KR_LANE - ROPE), w[:, 1184:]], axis=1)


def _w_in_from_kernel(g):
    return jnp.concatenate([g[:, 0:1152], g[:, Z_KR + KR_LANE:Z_KR + KR_LANE + ROPE], g[:, Z_GP:]], axis=1)


def _q_up_to_kernel(w):
    k = w.shape[0]
    return jnp.pad(w.reshape(k, NH, NOPE + ROPE), ((0, 0), (0, 0), (0, LANE - NOPE - ROPE))).reshape(k, NH * LANE)


def _q_up_from_kernel(g):
    k = g.shape[0]
    return g.reshape(k, NH, LANE)[:, :, :NOPE + ROPE].reshape(k, NH * (NOPE + ROPE))


def _kv_up_to_kernel(w):
    k = w.shape[0]
    w3 = w.reshape(k, NH, 2 * NOPE)
    kpart = jnp.pad(w3[:, :, :NOPE], ((0, 0), (0, 0), (0, LANE - NOPE))).reshape(k, NH * LANE)
    return jnp.concatenate([kpart, w3[:, :, NOPE:].reshape(k, NH * NOPE)], axis=1)


def _kv_up_from_kernel(g):
    k = g.shape[0]
    kpart = g[:, :NH * LANE].reshape(k, NH, LANE)[:, :, :NOPE]
    vpart = g[:, NH * LANE:].reshape(k, NH, NOPE)
    return jnp.concatenate([kpart, vpart], axis=2).reshape(k, NH * 2 * NOPE)


def _gain_slab(nope, rope):
    return jnp.concatenate([nope, rope, jnp.zeros((1, LANE - NOPE - ROPE), nope.dtype)], axis=1)


def _rope_tables(positions):
    inv_freq = 10000.0 ** (-jnp.arange(0, ROPE, 2, dtype=F32) / ROPE)
    ang = positions.astype(F32)[:, None] * inv_freq
    ang = jnp.concatenate([ang, ang], axis=-1)
    t = positions.shape[0]
    cos = jnp.concatenate([jnp.ones((t, KR_LANE), F32), jnp.cos(ang), jnp.ones((t, LANE - KR_LANE - ROPE), F32)], axis=1)
    sin = jnp.concatenate([jnp.zeros((t, KR_LANE), F32), jnp.sin(ang), jnp.zeros((t, LANE - KR_LANE - ROPE), F32)], axis=1)
    return cos, sin


def _coords():
    return lax.axis_index("x"), lax.axis_index("y"), lax.axis_index("c")


HBM_SPEC = pl.BlockSpec(memory_space=pl.ANY)
VMEM_SPEC = pl.BlockSpec(memory_space=pltpu.VMEM)


CHIP_RELS = ((1, 0), (0, 1), (1, 1))


def _gather_comm(shards):
    n = len(shards)

    def ici(ins, outs, sems, a, j, x, y, cc):
        half = ins[a].shape[0] // 2
        mine = pl.ds(cc * half, half)
        dx, dy = CHIP_RELS[j]
        return pltpu.make_async_remote_copy(src_ref=ins[a].at[mine], dst_ref=outs[a].at[2 * x + y, mine],
                                            send_sem=sems[0].at[a, j], recv_sem=sems[1].at[a, j],
                                            device_id=(x ^ dx, y ^ dy, cc), device_id_type=MESH)

    def d2d(ins, outs, sems, a, j, x, y, cc, half_of):
        half = ins[a].shape[0] // 2
        dx, dy = CHIP_RELS[j]
        landed = outs[a].at[2 * (x ^ dx) + (y ^ dy), pl.ds(half_of * half, half)]
        return pltpu.make_async_remote_copy(src_ref=landed, dst_ref=landed, send_sem=sems[2].at[a, j], recv_sem=sems[3].at[a, j],
                                            device_id=(x, y, 1 - cc), device_id_type=MESH)

    def own(ins, outs, sems, a, x, y):
        return pltpu.make_async_copy(ins[a], outs[a].at[2 * x + y], sems[4].at[a])

    def start(ins, outs, sems):
        x, y, cc = _coords()
        for a in range(n):
            own(ins, outs, sems, a, x, y).start()
            for j in range(3):
                ici(ins, outs, sems, a, j, x, y, cc).start()

    def finish(ins, outs, sems):
        x, y, cc = _coords()
        for a in range(n):
            for j in range(3):
                ici(ins, outs, sems, a, j, x, y, cc).wait_recv()
                d2d(ins, outs, sems, a, j, x, y, cc, cc).start()
        for a in range(n):
            for j in range(3):
                d2d(ins, outs, sems, a, j, x, y, cc, 1 - cc).wait_recv()
        for a in range(n):
            for j in range(3):
                ici(ins, outs, sems, a, j, x, y, cc).wait_send()
                d2d(ins, outs, sems, a, j, x, y, cc, cc).wait_send()
            own(ins, outs, sems, a, x, y).wait()

    dma = pltpu.SemaphoreType.DMA
    return _Comm(shards, [_sds((N_CHIP,) + s.shape, s.dtype) for s in shards],
                 [dma((n, 3)), dma((n, 3)), dma((n, 3)), dma((n, 3)), dma((n,))], start, finish)


def _swap_comm(parts):
    n = len(parts)

    def copy(ins, outs, sems, a):
        x, y, cc = _coords()
        half = ins[a].shape[1] // 2
        return pltpu.make_async_remote_copy(src_ref=ins[a].at[:, pl.ds((1 - cc) * half, half)], dst_ref=outs[a],
                                            send_sem=sems[0].at[a], recv_sem=sems[1].at[a], device_id=(x, y, 1 - cc),
                                            device_id_type=MESH)

    def start(ins, outs, sems):
        for a in range(n):
            copy(ins, outs, sems, a).start()

    def finish(ins, outs, sems):
        for a in range(n):
            copy(ins, outs, sems, a).wait()

    dma = pltpu.SemaphoreType.DMA
    return _Comm(parts, [_sds((p.shape[0], p.shape[1] // 2, p.shape[2]), p.dtype) for p in parts], [dma((n,)), dma((n,))],
                 start, finish)


def _add_half(full, other, cidx, *, name):
    nch, r, c = full.shape
    half = r // 2
    tr = _pick_rows(half)
    nbk = half // tr
    grid_spec = pltpu.PrefetchScalarGridSpec(
        num_scalar_prefetch=1, grid=(nch, nbk),
        in_specs=[pl.BlockSpec((1, tr, c), lambda j, i, cref: (j, cref[0] * nbk + i, 0)),
                  pl.BlockSpec((1, tr, c), lambda j, i, cref: (j, i, 0))],
        out_specs=pl.BlockSpec((1, tr, c), lambda j, i, cref: (j, i, 0)))

    def body(cref, a_ref, b_ref, o_ref):
        o_ref[...] = (a_ref[...] + b_ref[...]).astype(o_ref.dtype)

    return _pcall(body, name=name, out_shape=_sds((nch, half, c), MX), grid_spec=grid_spec)(cidx, full, other)


def _pick_rows(rows, target=512):
    best = None
    for t in range(16, min(rows, target) + 1, 16):
        if rows % t == 0:
            best = t
    return rows if best is None else best


def _exchange_comm(parts):
    n = len(parts)

    def send(ins, outs, sems, a, j, x, y, cc):
        dx, dy = CHIP_RELS[j]
        return pltpu.make_async_remote_copy(src_ref=ins[a].at[2 * (x ^ dx) + (y ^ dy)], dst_ref=outs[a].at[2 * x + y],
                                            send_sem=sems[0].at[a, j], recv_sem=sems[1].at[a, j],
                                            device_id=(x ^ dx, y ^ dy, cc), device_id_type=MESH)

    def landing(ins, outs, sems, a, j, x, y, cc):
        dx, dy = CHIP_RELS[j]
        peer_chip = 2 * (x ^ dx) + (y ^ dy)
        return pltpu.make_async_remote_copy(src_ref=ins[a].at[peer_chip], dst_ref=outs[a].at[peer_chip], send_sem=sems[0].at[a, j],
                                            recv_sem=sems[1].at[a, j], device_id=(x, y, cc), device_id_type=MESH)

    def own(ins, outs, sems, a, x, y):
        return pltpu.make_async_copy(ins[a].at[2 * x + y], outs[a].at[2 * x + y], sems[2].at[a])

    def start(ins, outs, sems):
        x, y, cc = _coords()
        for a in range(n):
            own(ins, outs, sems, a, x, y).start()
            for j in range(3):
                send(ins, outs, sems, a, j, x, y, cc).start()

    def finish(ins, outs, sems):
        x, y, cc = _coords()
        for a in range(n):
            for j in range(3):
                landing(ins, outs, sems, a, j, x, y, cc).wait_recv()
        for a in range(n):
            for j in range(3):
                send(ins, outs, sems, a, j, x, y, cc).wait_send()
            own(ins, outs, sems, a, x, y).wait()

    dma = pltpu.SemaphoreType.DMA
    return _Comm(parts, [_sds(p.shape, p.dtype) for p in parts], [dma((n, 3)), dma((n, 3)), dma((n,))], start, finish)


def _sum_chips(q, cidx, *, name):
    nch, h, c = q.shape
    tr = _pick_rows(h)
    nbk = h // tr
    grid_spec = pltpu.PrefetchScalarGridSpec(
        num_scalar_prefetch=1, grid=(nbk,),
        in_specs=[pl.BlockSpec((nch, tr, c), lambda i, cref: (0, i, 0))],
        out_specs=pl.BlockSpec((tr, c), lambda i, cref: (cref[0] * nbk + i, 0)))

    def body(cref, q_ref, o_ref):
        acc = q_ref[0].astype(F32) + q_ref[1].astype(F32)
        acc = acc + q_ref[2].astype(F32)
        o_ref[...] = acc + q_ref[3].astype(F32)

    return _pcall(body, name=name, out_shape=_sds((2 * h, c), F32), grid_spec=grid_spec)(cidx, q)


def _join_comm(fulls):
    n = len(fulls)

    def half_copy(outs, sems, a, which):
        x, y, cc = _coords()
        h = outs[a].shape[0] // 2
        rows = outs[a].at[pl.ds((cc if which == 0 else 1 - cc) * h, h)]
        return pltpu.make_async_remote_copy(src_ref=rows, dst_ref=rows, send_sem=sems[0].at[a], recv_sem=sems[1].at[a],
                                            device_id=(x, y, 1 - cc), device_id_type=MESH)

    def start(ins, outs, sems):
        for a in range(n):
            half_copy(outs, sems, a, 0).start()

    def finish(ins, outs, sems):
        for a in range(n):
            half_copy(outs, sems, a, 1).wait_recv()
        for a in range(n):
            half_copy(outs, sems, a, 0).wait_send()

    dma = pltpu.SemaphoreType.DMA
    return _Comm(fulls, [_sds(p.shape, p.dtype) for p in fulls], [dma((n,)), dma((n,))], start, finish,
                 aliases={a: a for a in range(n)})


def _ada_prologue(c, w, b, *, name, comm=None):
    nb, dm = c.shape
    n = w.shape[1]

    def body(c_ref, w_ref, b_ref, call_ref, land_ref, cpad, mod_s, g_send, g_recv, m_send, m_recv):
        x, y, cc = _coords()
        me = 4 * x + 2 * y + cc
        chip = 2 * x + y
        cpad[...] = jnp.zeros_like(cpad)
        cpad[0:nb, :] = c_ref[...]
        copies = []
        for kk in range(1, N_DEV):
            peer = (x ^ (kk >> 2), y ^ ((kk >> 1) & 1), cc ^ (kk & 1))
            cp = pltpu.make_async_remote_copy(src_ref=cpad, dst_ref=call_ref.at[me], send_sem=g_send.at[kk - 1],
                                              recv_sem=g_recv.at[kk - 1], device_id=peer, device_id_type=MESH)
            cp.start()
            copies.append(cp)
        call_ref[me] = cpad[...]
        for kk in range(1, N_DEV):
            pltpu.make_async_remote_copy(src_ref=cpad, dst_ref=call_ref.at[me ^ kk], send_sem=g_send.at[kk - 1],
                                         recv_sem=g_recv.at[kk - 1], device_id=(x, y, cc), device_id_type=MESH).wait_recv()
        cv = call_ref[...].reshape(N_DEV * SUBLANE, dm)
        act = (cv * _sigmoid(cv)).astype(MX)
        mod = jnp.dot(act, w_ref[...].astype(MX), preferred_element_type=F32) + b_ref[...]
        mod_s[...] = mod.reshape(N_DEV, SUBLANE, n)
        for j, (dx, dy) in enumerate(CHIP_RELS):
            cp = pltpu.make_async_remote_copy(src_ref=mod_s.at[4 * (x ^ dx) + 2 * (y ^ dy) + cc], dst_ref=land_ref.at[chip],
                                              send_sem=m_send.at[j], recv_sem=m_recv.at[j],
                                              device_id=(x ^ dx, y ^ dy, cc), device_id_type=MESH)
            cp.start()
            copies.append(cp)
        land_ref[chip] = mod_s[me]
        for j, (dx, dy) in enumerate(CHIP_RELS):
            pltpu.make_async_remote_copy(src_ref=mod_s.at[me], dst_ref=land_ref.at[2 * (x ^ dx) + (y ^ dy)], send_sem=m_send.at[j],
                                         recv_sem=m_recv.at[j], device_id=(x, y, cc), device_id_type=MESH).wait_recv()
        for cp in copies:
            cp.wait_send()

    dma = pltpu.SemaphoreType.DMA
    return _pcall(body, name=name, in_specs=[VMEM_SPEC] * 3, out_specs=[VMEM_SPEC] * 2,
                  out_shape=[_sds((N_DEV, SUBLANE, dm), F32), _sds((N_CHIP, SUBLANE, n), F32)],
                  scratch=[pltpu.VMEM((SUBLANE, dm), F32), pltpu.VMEM((N_DEV, SUBLANE, n), F32),
                           dma((N_DEV - 1,)), dma((N_DEV - 1,)), dma((3,)), dma((3,))], comm=comm)(c, w, b)


def _ada_bwd(c_all, dmod_cols, *, name):
    m, kdim = c_all.shape
    n = dmod_cols.shape[1]
    tn = _pick(n, 1152)

    def body(c_ref, d_ref, o_ref):
        cv = c_ref[...]
        act = (cv * _sigmoid(cv)).astype(MX)
        o_ref[...] = lax.dot_general(act, d_ref[...].astype(MX), (((0,), (0,)), ((), ())), preferred_element_type=F32)

    return _pcall(body, name=name, out_shape=_sds((kdim, n), F32), grid=(n // tn,),
                  in_specs=[pl.BlockSpec((m, kdim), lambda j: (0, 0)), pl.BlockSpec((m, tn), lambda j: (0, j))],
                  out_specs=pl.BlockSpec((kdim, tn), lambda j: (0, j)))(c_all, dmod_cols)


SLAB_W = 1024
RED_ROWS = 8


LOSS_LANE = SLAB_W - LANE


def _pack_stats(st1, st2, st3, st_fin, st_q, st_kv, st_prep, d_pool_scale, loss8, *, name):
    nb = st1.shape[0]
    dm_rows = -(-9 * nb // SUBLANE) * SUBLANE

    def body(s1, s2, s3, sf, sq, skv, sp, sps, loss_ref, o_ref):
        o_ref[...] = jnp.zeros_like(o_ref)
        for s in range(nb):
            rows = [s1[s, 0:1, :], s1[s, 1:2, :], s2[s, 3:4, :], s2[s, 0:1, :], s2[s, 1:2, :], s3[s, 3:4, :], s3[s, 0:1, :],
                    s3[s, 1:2, :], sf[s, 0:1, :]]
            for k, row in enumerate(rows):
                o_ref[9 * s + k:9 * s + k + 1, :] = row

        def over_seq(ref, r):
            acc = ref[0, r:r + 1, :]
            for s in range(1, nb):
                acc = acc + ref[s, r:r + 1, :]
            return acc

        o_ref[dm_rows + 0:dm_rows + 1, :] = over_seq(s1, 2)
        o_ref[dm_rows + 1:dm_rows + 2, :] = over_seq(s2, 2)
        o_ref[dm_rows + 2:dm_rows + 3, :] = over_seq(s3, 2)
        o_ref[dm_rows + 3:dm_rows + 4, 0:POOL_W] = over_seq(sps, 0)
        o_ref[dm_rows + 4:dm_rows + 5, 0:QL] = over_seq(sq, 0)
        o_ref[dm_rows + 5:dm_rows + 6, 0:KVL] = over_seq(skv, 0)
        o_ref[dm_rows + 6:dm_rows + 7, 0:LANE] = over_seq(sp, 0)
        o_ref[dm_rows + 7:dm_rows + 8, 0:LANE] = over_seq(sp, 1)
        o_ref[dm_rows + 7:dm_rows + 8, LOSS_LANE:] = loss_ref[0:1, :]

    return _pcall(body, name=name, out_shape=_sds((dm_rows + RED_ROWS, SLAB_W), F32), in_specs=[VMEM_SPEC] * 9,
                  out_specs=VMEM_SPEC)(st1, st2, st3, st_fin, st_q, st_kv, st_prep, d_pool_scale, loss8)


def _small_allreduce(slab, pool, *, name, comm=None):
    rows, w = slab.shape
    dm = rows - RED_ROWS
    prow, pw = pool.shape
    crow = RED_ROWS + 2 * dm

    def body(slab_ref, pool_ref, red_ref, dmod_ref, ptot_ref, sib_slab, sib_pool, chip_slab, chip_pool, land_slab, land_pool,
             a_send, a_recv, b_send, b_recv):
        x, y, cc = _coords()
        chip = 2 * x + y
        sib = (x, y, 1 - cc)
        to_sib = [pltpu.make_async_remote_copy(src_ref=slab_ref, dst_ref=sib_slab, send_sem=a_send.at[0], recv_sem=a_recv.at[0],
                                               device_id=sib, device_id_type=MESH),
                  pltpu.make_async_remote_copy(src_ref=pool_ref, dst_ref=sib_pool, send_sem=a_send.at[1], recv_sem=a_recv.at[1],
                                               device_id=sib, device_id_type=MESH)]
        for cp in to_sib:
            cp.start()
        for cp in to_sib:
            cp.wait()
        mine_dm, theirs_dm = slab_ref[0:dm, :], sib_slab[0:dm, :]
        chip_slab[0:RED_ROWS, :] = slab_ref[dm:, :] + sib_slab[dm:, :]
        chip_slab[RED_ROWS:RED_ROWS + dm, :] = jnp.where(cc == 0, mine_dm, theirs_dm)
        chip_slab[RED_ROWS + dm:, :] = jnp.where(cc == 0, theirs_dm, mine_dm)
        chip_pool[...] = pool_ref[...] + sib_pool[...]

        sends = []
        for j, (dx, dy) in enumerate(CHIP_RELS):
            peer = (x ^ dx, y ^ dy, cc)
            sends.append(pltpu.make_async_remote_copy(src_ref=chip_slab, dst_ref=land_slab.at[chip], send_sem=b_send.at[j, 0],
                                                      recv_sem=b_recv.at[j, 0], device_id=peer, device_id_type=MESH))
            sends.append(pltpu.make_async_remote_copy(src_ref=chip_pool, dst_ref=land_pool.at[chip], send_sem=b_send.at[j, 1],
                                                      recv_sem=b_recv.at[j, 1], device_id=peer, device_id_type=MESH))
        for cp in sends:
            cp.start()
        land_slab[chip] = chip_slab[...]
        land_pool[chip] = chip_pool[...]
        for j, (dx, dy) in enumerate(CHIP_RELS):
            peer_chip = 2 * (x ^ dx) + (y ^ dy)
            pltpu.make_async_remote_copy(src_ref=chip_slab, dst_ref=land_slab.at[peer_chip], send_sem=b_send.at[j, 0],
                                         recv_sem=b_recv.at[j, 0], device_id=(x, y, cc), device_id_type=MESH).wait_recv()
            pltpu.make_async_remote_copy(src_ref=chip_pool, dst_ref=land_pool.at[peer_chip], send_sem=b_send.at[j, 1],
                                         recv_sem=b_recv.at[j, 1], device_id=(x, y, cc), device_id_type=MESH).wait_recv()
        red = land_slab[0, 0:RED_ROWS, :]
        ptot = land_pool[0]
        for ch in range(1, N_CHIP):
            red = red + land_slab[ch, 0:RED_ROWS, :]
            ptot = ptot + land_pool[ch]
        red_ref[...] = red
        ptot_ref[...] = ptot
        for ch in range(N_CHIP):
            dmod_ref[2 * dm * ch:2 * dm * (ch + 1), :] = land_slab[ch, RED_ROWS:, :]
        for cp in sends:
            cp.wait_send()

    dma = pltpu.SemaphoreType.DMA
    return _pcall(body, name=name, in_specs=[VMEM_SPEC, VMEM_SPEC], out_specs=[VMEM_SPEC] * 3,
                  out_shape=[_sds((RED_ROWS, w), F32), _sds((N_DEV * dm, w), F32), _sds((prow, pw), F32)],
                  scratch=[pltpu.VMEM((rows, w), F32), pltpu.VMEM((prow, pw), F32), pltpu.VMEM((crow, w), F32),
                           pltpu.VMEM((prow, pw), F32), pltpu.VMEM((N_CHIP, crow, w), F32), pltpu.VMEM((N_CHIP, prow, pw), F32),
                           dma((2,)), dma((2,)), dma((3, 2)), dma((3, 2))], comm=comm)(slab, pool)


def _adamw_math(w, g, m, v):
    mn = ADAM_B1 * m + (1.0 - ADAM_B1) * g
    vn = ADAM_B2 * v + (1.0 - ADAM_B2) * (g * g)
    bc1 = 1.0 / (1.0 - ADAM_B1 ** ADAM_STEP)
    bc2 = 1.0 / (1.0 - ADAM_B2 ** ADAM_STEP)
    return -ADAM_LR * ((mn * bc1) / (jnp.sqrt(vn * bc2) + ADAM_EPS) + ADAM_WD * w), mn, vn


def _small_update(red, dmod_all, pool_total, nb, params, *, name):
    names = list(SMALL)
    dm = dmod_all.shape[0] // N_DEV

    def grad_of(nm, red_ref, dmod_ref, ptot_ref):
        if nm == "b_ada":
            acc = None
            for d in range(N_DEV):
                for s in range(nb):
                    blk = dmod_ref[d * dm + 9 * s:d * dm + 9 * s + 9, :]
                    acc = blk if acc is None else acc + blk
            return jnp.concatenate([acc[k:k + 1, :] for k in range(9)], axis=1)
        if nm == "pool_grp":
            return ptot_ref[...]
        row, lo, n = {"norm_ffn1": (0, 0, D), "norm_mix": (1, 0, D), "norm_ffn2": (2, 0, D), "pool_scale": (3, 0, POOL_W),
                      "q_a_norm": (4, 0, QL), "kv_a_norm": (5, 0, KVL), "q_norm_nope": (6, 0, NOPE),
                      "q_norm_rope": (6, NOPE, ROPE), "k_norm_nope": (7, 0, NOPE), "k_norm_rope": (7, KR_LANE, ROPE)}[nm]
        return red_ref[row:row + 1, lo:lo + n]

    def body(*refs):
        red_ref, dmod_ref, ptot_ref = refs[:3]
        ins = refs[3:3 + 3 * len(names)]
        outs = refs[3 + 3 * len(names):]
        outs[4 * len(names)][...] = red_ref[RED_ROWS - 1:RED_ROWS, LOSS_LANE:]
        for i, nm in enumerate(names):
            g = grad_of(nm, red_ref, dmod_ref, ptot_ref)
            d, mn, vn = _adamw_math(ins[3 * i][...], g, ins[3 * i + 1][...], ins[3 * i + 2][...])
            outs[4 * i][...] = g
            outs[4 * i + 1][...] = d
            outs[4 * i + 2][...] = mn
            outs[4 * i + 3][...] = vn

    flat_in = [a for nm in names for a in params[nm]]
    out_shape = [_sds(params[nm][0].shape, F32) for nm in names for _ in range(4)] + [_sds((1, LANE), F32)]
    res = _pcall(body, name=name, in_specs=[VMEM_SPEC] * (3 + len(flat_in)), out_specs=[VMEM_SPEC] * len(out_shape),
                 out_shape=out_shape)(red, dmod_all, pool_total, *flat_in)
    return {nm: tuple(res[4 * i:4 * i + 4]) for i, nm in enumerate(names)}, res[-1]


def _adamw(w, g, m, v, *, name, comm=None):
    r, c = w.shape
    tr = _pick_rows(r, 256)
    tc = c if tr < r else _pick(c, 256)
    spec = pl.BlockSpec((tr, tc), lambda i, j: (i, j))
    bc1 = 1.0 / (1.0 - ADAM_B1 ** ADAM_STEP)
    bc2 = 1.0 / (1.0 - ADAM_B2 ** ADAM_STEP)

    def body(w_ref, g_ref, m_ref, v_ref, d_ref, mo_ref, vo_ref):
        gv = g_ref[...]
        mn = ADAM_B1 * m_ref[...] + (1.0 - ADAM_B1) * gv
        vn = ADAM_B2 * v_ref[...] + (1.0 - ADAM_B2) * (gv * gv)
        mo_ref[...] = mn
        vo_ref[...] = vn
        d_ref[...] = -ADAM_LR * ((mn * bc1) / (jnp.sqrt(vn * bc2) + ADAM_EPS) + ADAM_WD * w_ref[...])

    out = _sds((r, c), F32)
    return _pcall(body, name=name, out_shape=[out, out, out], grid=(r // tr, c // tc), in_specs=[spec] * 4, out_specs=[spec] * 3,
                  comm=comm)(w, g, m, v)


BIG = ("w_ffn1_in", "w_ffn1_out", "w_in", "w_pool_proj", "w_q_up", "w_kv_up", "w_mla_proj", "w_out", "w_ffn2_in", "w_ffn2_out")
ROW_SHARDED = ("w_ffn1_out", "w_out", "w_ffn2_out")
KERNEL_NAME = {"w_ffn1_in": "ffn1_in", "w_ffn1_out": "ffn1_out", "w_in": "w_in", "w_pool_proj": "pool_proj", "w_q_up": "q_up",
               "w_kv_up": "kv_up", "w_mla_proj": "mla_proj", "w_out": "w_out", "w_ffn2_in": "ffn2_in", "w_ffn2_out": "ffn2_out"}
WEIGHTS = ("w_ada", "b_ada", "norm_ffn1", "w_ffn1_in", "w_ffn1_out", "norm_mix", "w_in", "pool_grp", "pool_scale", "w_pool_proj",
           "q_a_norm", "w_q_up", "kv_a_norm", "w_kv_up", "q_norm_nope", "q_norm_rope", "k_norm_nope", "k_norm_rope", "w_mla_proj",
           "w_out", "norm_ffn2", "w_ffn2_in", "w_ffn2_out")
SMALL = ("b_ada", "norm_ffn1", "norm_mix", "pool_grp", "pool_scale", "q_a_norm", "kv_a_norm", "q_norm_nope", "q_norm_rope",
         "k_norm_nope", "k_norm_rope", "norm_ffn2")


def _assemble(name, stacked):
    if name in ROW_SHARDED:
        return stacked.reshape(stacked.shape[0] * stacked.shape[1], stacked.shape[2])
    return jnp.transpose(stacked, (1, 0, 2)).reshape(stacked.shape[1], stacked.shape[0] * stacked.shape[2])


def _split(name, full):
    if name in ROW_SHARDED:
        return full.reshape(N_CHIP, full.shape[0] // N_CHIP, full.shape[1])
    return jnp.transpose(full.reshape(full.shape[0], N_CHIP, full.shape[1] // N_CHIP), (1, 0, 2))


GU = ("w_ffn1_in", "w_ffn2_in")
NARROW = ("w_in", "w_q_up")


MIX_SMALL = ("w_out", "w_pool_proj", "w_mla_proj", "w_q_up", "w_kv_up")
RIDES = {
    "ada_prologue": (("gather", ("w_ffn1_in",)),),
    "ffn1_up_a": (("gather", ("w_ffn1_out",)),),
    "ffn1_up_b": (("gather", ("w_in",)),),
    "mix_in": (("gather", MIX_SMALL),),
    "attn_fwd": (("gather", ("w_ffn2_in", "w_ffn2_out")),),
    "d_ffn2_h": (("swap", ("w_ffn2_out", "w_ffn2_in")),),
    "attn_bwd": (("exchange", ("w_ffn2_out", "w_ffn2_in")),),
    "d_mix_in": (("swap", MIX_SMALL),),
    "d_mix_h": (("exchange", MIX_SMALL), ("swap", ("w_in",))),
    "ffn1_dact": (("exchange", ("w_in",)), ("swap", ("w_ffn1_out",))),
    "d_ffn1_in_a": (("exchange", ("w_ffn1_out",)),),
    "d_ffn1_in_b": (("swap", ("w_ffn1_in@0",)),),
    "d_ffn1_h": (("exchange", ("w_ffn1_in@0",)), ("swap", ("w_ffn1_in@1",))),
    "bwd_norm1": (("exchange", ("w_ffn1_in@1",)),),
    "small_allreduce": (("join", tuple(n for n in BIG if n != "w_ffn1_in") + ("w_ffn1_in@0", "w_ffn1_in@1")),),
}


def _kname(n):
    base, _, part = n.partition("@")
    return KERNEL_NAME[base] + ("@" + part if part else "")


def _both(comms):
    if len(comms) == 1:
        return comms[0]
    ins, outs, sems, aliases, spans = [], [], [], {}, []
    for c in comms:
        spans.append((len(ins), len(c.ins), len(outs), len(c.out_shapes), len(sems), len(c.sems)))
        aliases.update({len(ins) + i: len(outs) + o for i, o in c.aliases.items()})
        ins, outs, sems = ins + c.ins, outs + c.out_shapes, sems + c.sems

    def each(which):
        def run(i_, o_, s_):
            for c, (ia, ni, oa, no, sa, ns) in zip(comms, spans):
                getattr(c, which)(i_[ia:ia + ni], o_[oa:oa + no], s_[sa:sa + ns])
        return run

    return _Comm(ins, outs, sems, each("start"), each("finish"), aliases)


class _ExchangePlan:
    def __init__(self, shards, cidx):
        self.shards, self.cidx = shards, cidx
        self.W, self.G, self.parts, self.pre, self.reduced, self.joined = {}, {}, {}, {}, {}, {}

    def grad(self, key, g):
        self.G[key] = g

    def rider(self, name):
        comms = []
        for kind, names in RIDES.get(name, ()):
            if kind == "gather":
                comms.append(_gather_comm([self.shards[n] for n in names]))
            elif kind == "swap":
                for n in names:
                    self.parts[n] = self._stacked(n)
                comms.append(_swap_comm([self.parts[n] for n in names]))
            elif kind == "exchange":
                comms.append(_exchange_comm([self.pre[n] for n in names]))
            else:
                comms.append(_join_comm([self.reduced[n] for n in names]))
        return _both(comms) if comms else None

    def landed(self, name, outs):
        at = 0
        for kind, names in RIDES[name]:
            for n, o in zip(names, outs[at:at + len(names)]):
                if kind == "gather":
                    self.W[KERNEL_NAME[n]] = self._to_kernel(n, o)
                elif kind == "swap":
                    self.pre[n] = _add_half(self.parts[n], o, self.cidx, name="rs_add_" + _kname(n))
                elif kind == "exchange":
                    self.reduced[n] = _sum_chips(o, self.cidx, name="rs_sum_" + _kname(n))
                else:
                    self.joined[n] = o
            at += len(names)

    @staticmethod
    def _to_kernel(n, stacked):
        if n in GU:
            return stacked
        full = _assemble(n, stacked)
        return {"w_in": _w_in_to_kernel, "w_q_up": _q_up_to_kernel, "w_kv_up": _kv_up_to_kernel}.get(n, lambda w: w)(full)

    def _stacked(self, n):
        g = self.G[_kname(n)]
        if n.partition("@")[0] in GU:
            return g
        full = {"w_in": _w_in_from_kernel, "w_q_up": _q_up_from_kernel, "w_kv_up": _kv_up_from_kernel}.get(n, lambda w: w)(g)
        return _split(n, full)


def kernel(x, c, positions, w_ada, b_ada, norm_ffn1, w_ffn1_in, w_ffn1_out, norm_mix, w_in, pool_grp, pool_scale, w_pool_proj, q_a_norm, w_q_up, kv_a_norm, w_kv_up, q_norm_nope, q_norm_rope, k_norm_nope, k_norm_rope, w_mla_proj, w_out, norm_ffn2, w_ffn2_in, w_ffn2_out, loss_target, m_w_ada, m_b_ada, m_norm_ffn1, m_w_ffn1_in, m_w_ffn1_out, m_norm_mix, m_w_in, m_pool_grp, m_pool_scale, m_w_pool_proj, m_q_a_norm, m_w_q_up, m_kv_a_norm, m_w_kv_up, m_q_norm_nope, m_q_norm_rope, m_k_norm_nope, m_k_norm_rope, m_w_mla_proj, m_w_out, m_norm_ffn2, m_w_ffn2_in, m_w_ffn2_out, v_w_ada, v_b_ada, v_norm_ffn1, v_w_ffn1_in, v_w_ffn1_out, v_norm_mix, v_w_in, v_pool_grp, v_pool_scale, v_w_pool_proj, v_q_a_norm, v_w_q_up, v_kv_a_norm, v_w_kv_up, v_q_norm_nope, v_q_norm_rope, v_k_norm_nope, v_k_norm_rope, v_w_mla_proj, v_w_out, v_norm_ffn2, v_w_ffn2_in, v_w_ffn2_out):
    args = dict(locals())
    wts = {n: args[n][0] for n in WEIGHTS}
    mom = {n: args["m_" + n][0] for n in WEIGHTS}
    var = {n: args["v_" + n][0] for n in WEIGHTS}
    nb, seq, dm = x.shape
    tokens = nb * seq
    xi, yi, ci = _coords()
    chip = 2 * xi + yi

    cidx = ci.astype(jnp.int32).reshape(1)
    plan = _ExchangePlan({n: wts[n].astype(MX) for n in BIG}, cidx)
    plan.W["pool_grp"] = wts["pool_grp"].astype(MX)

    ncol = w_ada.shape[2]
    b_cols = lax.dynamic_slice_in_dim(wts["b_ada"].reshape(1, -1), chip * ncol, ncol, axis=1)
    c_slots, mod_slots = _run(plan, _ada_prologue, c, wts["w_ada"], b_cols, name="ada_prologue")
    c_all = c_slots[:, :nb].reshape(N_DEV * nb, dm)
    mod3 = jnp.transpose(mod_slots[:, :nb], (1, 0, 2)).reshape(nb, 9, dm)
    P = {"norm_ffn1": wts["norm_ffn1"].reshape(1, dm), "norm_mix": wts["norm_mix"].reshape(1, dm),
         "norm_ffn2": wts["norm_ffn2"].reshape(1, dm), "pool_scale": wts["pool_scale"].reshape(1, POOL_W),
         "q_a_norm": wts["q_a_norm"].reshape(1, QL), "kv_a_norm": wts["kv_a_norm"].reshape(1, KVL),
         "q_gain": _gain_slab(wts["q_norm_nope"].reshape(1, NOPE), wts["q_norm_rope"].reshape(1, ROPE)),
         "k_gain": _gain_slab(wts["k_norm_nope"].reshape(1, NOPE), wts["k_norm_rope"].reshape(1, ROPE))}
    cos, sin = _rope_tables(positions.reshape(tokens))

    loss8, grad_x, stats, d_pool_grp = _layer_fwd_bwd(x.reshape(tokens, dm), loss_target.reshape(tokens, dm), mod3, cos, sin, plan, P)

    slab = _pack_stats(*stats, loss8, name="pack_stats")
    red, dmod_rows, pool_total = _run(plan, _small_allreduce, slab, d_pool_grp.reshape(POOL_G * LANE, LANE), name="small_allreduce")
    grads = {n: plan.joined[n] for n in BIG if n != "w_ffn1_in"}
    grads["w_ffn1_in"] = jnp.concatenate([plan.joined["w_ffn1_in@0"], plan.joined["w_ffn1_in@1"]], axis=0)
    dm_rows = dmod_rows.shape[0] // N_DEV
    dmod_all = dmod_rows.reshape(N_DEV, dm_rows, SLAB_W)[:, :9 * nb].reshape(N_DEV * nb, 9 * dm)
    dmod_cols = lax.dynamic_slice_in_dim(dmod_all, chip * ncol, ncol, axis=1)
    grads["w_ada"] = _ada_bwd(c_all, dmod_cols, name="ada_bwd")

    delta, new_m, new_v = {}, {}, {}
    as2d = lambda a: a.reshape(POOL_G * LANE, LANE) if a.ndim == 4 else a.reshape(1, -1)
    upd, loss_row = _small_update(red, dmod_rows, pool_total, nb,
                                  {n: tuple(as2d(args[p + n]) for p in ("", "m_", "v_")) for n in SMALL}, name="small_update")
    loss = loss_row[0, 0]
    for n in SMALL:
        grads[n], delta[n], new_m[n], new_v[n] = upd[n]
    for n in ("w_ada",) + BIG:
        if n in NARROW:
            res = _adamw(wts[n].T, grads[n].T, mom[n].T, var[n].T, name="adamw_" + n)
            delta[n], new_m[n], new_v[n] = (r.T for r in res)
        else:
            delta[n], new_m[n], new_v[n] = _adamw(wts[n], grads[n], mom[n], var[n], name="adamw_" + n)

    def lead(a, n):
        return a.reshape((1,) + wts[n].shape)

    return (loss, grad_x.reshape(nb, seq, dm), *[lead(grads[n], n) for n in WEIGHTS], *[lead(delta[n], n) for n in WEIGHTS],
            *[lead(new_m[n], n) for n in WEIGHTS], *[lead(new_v[n], n) for n in WEIGHTS])
```

```python
import functools
import math

import jax
import jax.numpy as jnp
from jax import lax
from jax.experimental import pallas as pl
from jax.experimental.pallas import tpu as pltpu

F32 = jnp.float32
MX = jnp.bfloat16

D = 1024
DFF = 2816
NH = 8
POOL_W = 512
POOL_G = 4
QL = 384
KVL = 256
ROPE = 32
NOPE = 64
LANE = 128
SUBLANE = 8
EPS = 1e-6
ATTN_SCALE = 1.0 / math.sqrt(96.0)
NEG = -1e30

Z_UP, Z_QL, Z_KV, Z_KR, Z_GP, Z_GM, Z_W = 0, 512, 896, 1152, 1280, 2304, 3328
KR_LANE = 64
LAT_W = 1280

ADAM_LR, ADAM_B1, ADAM_B2, ADAM_EPS, ADAM_WD, ADAM_STEP = 0.001, 0.9, 0.999, 1e-08, 0.01, 10

VMEM_LIMIT = 48 * 1024 * 1024
MESH = pl.DeviceIdType.MESH
N_DEV = 8
N_CHIP = 4


class _Comm:
    def __init__(self, ins, out_shapes, sems, start, finish, aliases=None):
        self.ins, self.out_shapes, self.sems = list(ins), list(out_shapes), list(sems)
        self.start, self.finish = start, finish
        self.aliases = aliases or {}


def _pcall(body, *, name, out_shape, grid=(), in_specs=None, out_specs=None, scratch=(), grid_spec=None, aliases=None,
           comm=None):
    params = pltpu.CompilerParams(vmem_limit_bytes=VMEM_LIMIT)
    kw = dict(name=name, compiler_params=params)
    if comm is None:
        if aliases:
            kw["input_output_aliases"] = aliases
        if grid_spec is not None:
            return pl.pallas_call(body, grid_spec=grid_spec, out_shape=out_shape, **kw)
        return pl.pallas_call(body, grid=grid, in_specs=in_specs, out_specs=out_specs, scratch_shapes=scratch,
                              out_shape=out_shape, **kw)
    single = not isinstance(out_shape, (list, tuple))
    outs = [out_shape] if single else list(out_shape)
    ospecs = [out_specs] if single else list(out_specs)
    n_in, n_out, n_ci, n_co, n_scr = len(in_specs), len(outs), len(comm.ins), len(comm.out_shapes), len(scratch)
    io = dict(aliases or {})
    io.update({n_in + i: n_out + o for i, o in comm.aliases.items()})

    def riding(*refs):
        ins, cins = refs[:n_in], refs[n_in:n_in + n_ci]
        at = n_in + n_ci
        os_, couts = refs[at:at + n_out], refs[at + n_out:at + n_out + n_co]
        at += n_out + n_co
        scr, csems = refs[at:at + n_scr], refs[at + n_scr:]
        if grid:
            first = functools.reduce(jnp.logical_and, [pl.program_id(d) == 0 for d in range(len(grid))])
            last = functools.reduce(jnp.logical_and, [pl.program_id(d) == grid[d] - 1 for d in range(len(grid))])
            pl.when(first)(lambda: comm.start(cins, couts, csems))
            body(*ins, *os_, *scr)
            pl.when(last)(lambda: comm.finish(cins, couts, csems))
        else:
            comm.start(cins, couts, csems)
            body(*ins, *os_, *scr)
            comm.finish(cins, couts, csems)

    call = pl.pallas_call(riding, grid=grid, in_specs=list(in_specs) + [HBM_SPEC] * n_ci, out_specs=ospecs + [HBM_SPEC] * n_co,
                          out_shape=outs + comm.out_shapes, scratch_shapes=list(scratch) + comm.sems,
                          input_output_aliases=io, **kw)

    def run(*args):
        res = call(*args, *comm.ins)
        main = list(res[:n_out])
        return (main[0] if single else main), list(res[n_out:])

    return run


def _pick(dim, target):
    best = None
    for t in range(LANE, min(dim, target) + 1, LANE):
        if dim % t == 0:
            best = t
    return dim if best is None else best


def _sds(shape, dtype):
    return jax.ShapeDtypeStruct(shape, dtype)


def _dw(a, g, *, name, tm=512, tn=1024, comm=None):
    return _mm(a, g, mode="tn", out_dtype=F32, name=name, tm=tm, tn=tn, tk=a.shape[0], n_outer=True, comm=comm)


def _mm(a, b, *, mode, out_dtype, name, tm=1024, tn=1024, tk=4096, n_outer=False, comm=None):
    if mode == "nn":
        (M, K), (K2, N) = a.shape, b.shape
    elif mode == "nt":
        (M, K), (N, K2) = a.shape, b.shape
    else:
        (K, M), (K2, N) = a.shape, b.shape
    assert K == K2, (name, a.shape, b.shape)
    tm, tn, tk = _pick(M, tm), _pick(N, tn), _pick(K, tk)
    nk = K // tk
    if n_outer:
        ij = lambda g0, g1: (g1, g0)
        grid = (N // tn, M // tm, nk)
    else:
        ij = lambda g0, g1: (g0, g1)
        grid = (M // tm, N // tn, nk)
    if mode == "tn":
        a_spec = pl.BlockSpec((tk, tm), lambda g0, g1, k: (k, ij(g0, g1)[0]))
    else:
        a_spec = pl.BlockSpec((tm, tk), lambda g0, g1, k: (ij(g0, g1)[0], k))
    if mode == "nt":
        b_spec = pl.BlockSpec((tn, tk), lambda g0, g1, k: (ij(g0, g1)[1], k))
    else:
        b_spec = pl.BlockSpec((tk, tn), lambda g0, g1, k: (k, ij(g0, g1)[1]))
    o_spec = pl.BlockSpec((tm, tn), lambda g0, g1, k: ij(g0, g1))
    dn = {"nn": (((1,), (0,)), ((), ())), "nt": (((1,), (1,)), ((), ())), "tn": (((0,), (0,)), ((), ()))}[mode]

    def dot(a_ref, b_ref):
        return lax.dot_general(a_ref[...].astype(MX), b_ref[...].astype(MX), dn, preferred_element_type=F32)

    def body_one(a_ref, b_ref, o_ref):
        o_ref[...] = dot(a_ref, b_ref).astype(o_ref.dtype)

    def body_acc(a_ref, b_ref, o_ref, acc_ref):
        k = pl.program_id(2)
        part = dot(a_ref, b_ref)

        @pl.when(k == 0)
        def _():
            acc_ref[...] = part

        @pl.when(k > 0)
        def _():
            acc_ref[...] += part

        @pl.when(k == nk - 1)
        def _():
            o_ref[...] = acc_ref[...].astype(o_ref.dtype)

    return _pcall(body_one if nk == 1 else body_acc, name=name, out_shape=_sds((M, N), out_dtype), grid=grid,
                  in_specs=[a_spec, b_spec], out_specs=o_spec, scratch=[] if nk == 1 else [pltpu.VMEM((tm, tn), F32)],
                  comm=comm)(a, b)


def _gu_shard(q):
    return (q % 2) * 2 + q // 2


def _ffn_up(h, w_st, *, name, tm=512, comm=None):
    T, dm = h.shape
    hw = w_st.shape[2]
    tm = _pick(T, tm)

    def body(h_ref, wg_ref, wu_ref, gu_ref, a_ref):
        hv = h_ref[...]
        g = jnp.dot(hv, wg_ref[0], preferred_element_type=F32)
        u = jnp.dot(hv, wu_ref[0], preferred_element_type=F32)
        gu_ref[:, :hw] = g.astype(gu_ref.dtype)
        gu_ref[:, hw:] = u.astype(gu_ref.dtype)
        a_ref[...] = (g * _sigmoid(g) * u).astype(a_ref.dtype)

    return _pcall(body, name=name, grid=(T // tm, 2),
                  in_specs=[pl.BlockSpec((tm, dm), lambda i, j: (i, 0)), pl.BlockSpec((1, dm, hw), lambda i, j: (j, 0, 0)),
                            pl.BlockSpec((1, dm, hw), lambda i, j: (2 + j, 0, 0))],
                  out_specs=[pl.BlockSpec((tm, 2 * hw), lambda i, j: (i, j)), pl.BlockSpec((tm, hw), lambda i, j: (i, j))],
                  out_shape=[_sds((T, 4 * hw), MX), _sds((T, 2 * hw), MX)], comm=comm)(h, w_st, w_st)


def _ffn_up_first(x, mod3, gain, w_st, *, sub, name, tm=512, comm=None):
    T, dm = x.shape
    hw = w_st.shape[2]
    tm = _pick(T, tm)
    tps = (T // mod3.shape[0]) // tm

    def body(x_ref, mod_ref, n_ref, wg_ref, wu_ref, h_ref, gu_ref, a_ref):
        xv = x_ref[...]
        xn = xv * _rsq(xv) * n_ref[...]
        hv = (xn * (1.0 + mod_ref[0, 3 * sub + 1:3 * sub + 2, :]) + mod_ref[0, 3 * sub:3 * sub + 1, :]).astype(h_ref.dtype)
        h_ref[...] = hv
        g = jnp.dot(hv, wg_ref[0], preferred_element_type=F32)
        u = jnp.dot(hv, wu_ref[0], preferred_element_type=F32)
        gu_ref[:, :hw] = g.astype(gu_ref.dtype)
        gu_ref[:, hw:] = u.astype(gu_ref.dtype)
        a_ref[...] = (g * _sigmoid(g) * u).astype(a_ref.dtype)

    return _pcall(body, name=name, grid=(T // tm,),
                  in_specs=[_row_spec(tm, dm), pl.BlockSpec((1, 9, dm), lambda i: (i // tps, 0, 0)),
                            pl.BlockSpec((1, dm), lambda i: (0, 0)), pl.BlockSpec((1, dm, hw), lambda i: (0, 0, 0)),
                            pl.BlockSpec((1, dm, hw), lambda i: (2, 0, 0))],
                  out_specs=[_row_spec(tm, dm), pl.BlockSpec((tm, 2 * hw), lambda i: (i, 0)), pl.BlockSpec((tm, hw), lambda i: (i, 0))],
                  out_shape=[_sds((T, dm), MX), _sds((T, 4 * hw), MX), _sds((T, 2 * hw), MX)],
                  comm=comm)(x, mod3, gain, w_st, w_st)


def _ffn_up_second(h, w_st, gu, a, *, name, tm=512, comm=None):
    T, dm = h.shape
    hw = w_st.shape[2]
    tm = _pick(T, tm)

    def body(h_ref, wg_ref, wu_ref, gu_in, a_in, gu_ref, a_ref):
        hv = h_ref[...]
        g = jnp.dot(hv, wg_ref[0], preferred_element_type=F32)
        u = jnp.dot(hv, wu_ref[0], preferred_element_type=F32)
        gu_ref[:, :hw] = g.astype(gu_ref.dtype)
        gu_ref[:, hw:] = u.astype(gu_ref.dtype)
        a_ref[...] = (g * _sigmoid(g) * u).astype(a_ref.dtype)

    return _pcall(body, name=name, grid=(T // tm,),
                  in_specs=[_row_spec(tm, dm), pl.BlockSpec((1, dm, hw), lambda i: (1, 0, 0)),
                            pl.BlockSpec((1, dm, hw), lambda i: (3, 0, 0)), HBM_SPEC, HBM_SPEC],
                  out_specs=[pl.BlockSpec((tm, 2 * hw), lambda i: (i, 1)), pl.BlockSpec((tm, hw), lambda i: (i, 1))],
                  out_shape=[_sds(gu.shape, gu.dtype), _sds(a.shape, a.dtype)], aliases={3: 0, 4: 1},
                  comm=comm)(h, w_st, w_st, gu, a)


def _ffn_dact(df, gu, w_out, *, name, tm=512, comm=None):
    T, dm = df.shape
    hw = gu.shape[1] // 4
    tm = _pick(T, tm)

    def body(df_ref, gu_ref, wo_ref, dgu_ref):
        da = lax.dot_general(df_ref[...], wo_ref[...], (((1,), (1,)), ((), ())), preferred_element_type=F32)
        g = gu_ref[:, :hw].astype(F32)
        u = gu_ref[:, hw:].astype(F32)
        s = _sigmoid(g)
        dgu_ref[:, :hw] = (da * u * (s * (1.0 + g * (1.0 - s)))).astype(dgu_ref.dtype)
        dgu_ref[:, hw:] = (da * (g * s)).astype(dgu_ref.dtype)

    return _pcall(body, name=name, grid=(T // tm, 2),
                  in_specs=[pl.BlockSpec((tm, dm), lambda i, j: (i, 0)), pl.BlockSpec((tm, 2 * hw), lambda i, j: (i, j)),
                            pl.BlockSpec((hw, dm), lambda i, j: (j, 0))],
                  out_specs=pl.BlockSpec((tm, 2 * hw), lambda i, j: (i, j)), out_shape=_sds(gu.shape, MX),
                  comm=comm)(df, gu, w_out)


def _ffn_dh(dgu, w_st, *, name, tm=1024, comm=None):
    T = dgu.shape[0]
    _, dm, hw = w_st.shape
    tm = _pick(T, tm)

    def body(d_ref, w_ref, o_ref, acc_ref):
        q = pl.program_id(1)
        part = lax.dot_general(d_ref[...], w_ref[0], (((1,), (1,)), ((), ())), preferred_element_type=F32)

        @pl.when(q == 0)
        def _():
            acc_ref[...] = part

        @pl.when(jnp.logical_and(q > 0, q < 3))
        def _():
            acc_ref[...] += part

        @pl.when(q == 3)
        def _():
            o_ref[...] = acc_ref[...] + part

    return _pcall(body, name=name, grid=(T // tm, 4),
                  in_specs=[pl.BlockSpec((tm, hw), lambda i, q: (i, q)), pl.BlockSpec((1, dm, hw), lambda i, q: (_gu_shard(q), 0, 0))],
                  out_specs=pl.BlockSpec((tm, dm), lambda i, q: (i, 0)), out_shape=_sds((T, dm), F32),
                  scratch=[pltpu.VMEM((tm, dm), F32)], comm=comm)(dgu, w_st)


def _ffn_dw_in(h, dgu, *, name, rows=None, tm=512, comm=None):
    T, dm = h.shape
    hw = dgu.shape[1] // 4
    first, count = rows if rows is not None else (0, dm)
    tm = _pick(count, tm)
    skip = first // tm

    def body(h_ref, d_ref, o_ref):
        o_ref[0] = lax.dot_general(h_ref[...], d_ref[...], (((0,), (0,)), ((), ())), preferred_element_type=F32)

    return _pcall(body, name=name, grid=(4, count // tm),
                  in_specs=[pl.BlockSpec((T, tm), lambda q, i: (0, skip + i)), pl.BlockSpec((T, hw), lambda q, i: (0, q))],
                  out_specs=pl.BlockSpec((1, tm, hw), lambda q, i: (_gu_shard(q), i, 0)),
                  out_shape=_sds((4, count, hw), F32), comm=comm)(h, dgu)


def _rsq(x):
    return lax.rsqrt(jnp.mean(x * x, axis=-1, keepdims=True) + EPS)


def _sigmoid(x):
    return 1.0 / (1.0 + jnp.exp(-x))


def _row_spec(tm, w):
    return pl.BlockSpec((tm, w), lambda i: (i, 0))


def _mm_resid_norm(a, w, x_prev, mod3, gain, *, sub, coef, name, tm=512, comm=None):
    T, k = a.shape
    dm = w.shape[1]
    tm = _pick(T, tm)
    tps = (T // mod3.shape[0]) // tm

    def body(a_ref, w_ref, x_ref, mod_ref, n_ref, f_ref, xo_ref, h_ref):
        f = jnp.dot(a_ref[...], w_ref[...], preferred_element_type=F32)
        f_ref[...] = f
        x = x_ref[...] + coef * mod_ref[0, 3 * sub - 1:3 * sub, :] * f
        xo_ref[...] = x
        xn = x * _rsq(x) * n_ref[...]
        h_ref[...] = (xn * (1.0 + mod_ref[0, 3 * sub + 1:3 * sub + 2, :]) + mod_ref[0, 3 * sub:3 * sub + 1, :]).astype(h_ref.dtype)

    row = _row_spec(tm, dm)
    return _pcall(body, name=name, grid=(T // tm,),
                  in_specs=[_row_spec(tm, k), pl.BlockSpec((k, dm), lambda i: (0, 0)), row,
                            pl.BlockSpec((1, 9, dm), lambda i: (i // tps, 0, 0)), pl.BlockSpec((1, dm), lambda i: (0, 0))],
                  out_specs=[row, row, row], out_shape=[_sds((T, dm), F32), _sds((T, dm), F32), _sds((T, dm), MX)],
                  comm=comm)(a, w, x_prev, mod3, gain)


def _mm_loss(a, w, x2, tgt, mod3, *, name, tm=512):
    T, k = a.shape
    dm = w.shape[1]
    tm = _pick(T, tm)
    tps = (T // mod3.shape[0]) // tm
    mod_spec = pl.BlockSpec((1, 9, dm), lambda i: (i // tps, 0, 0))
    row = _row_spec(tm, dm)
    stat_spec = pl.BlockSpec((1, SUBLANE, dm), lambda i: (i // tps, 0, 0))
    loss_spec = pl.BlockSpec((SUBLANE, LANE), lambda i: (0, 0))

    def body(a_ref, w_ref, x_ref, t_ref, mod_ref, dy_ref, df_ref, st_ref, loss_ref):
        i = pl.program_id(0)
        g = mod_ref[0, 8:9, :]
        f = jnp.dot(a_ref[...], w_ref[...], preferred_element_type=F32)
        err = x_ref[...] + 0.5 * g * f - t_ref[...]
        dy = err * (1.0 / dm)
        dy_ref[...] = dy
        df_ref[...] = (0.5 * g * dy).astype(df_ref.dtype)
        dgate = jnp.sum(0.5 * dy * f, axis=0, keepdims=True)
        part = 0.5 * jnp.sum(jnp.sum(err * err, axis=0, keepdims=True), axis=1, keepdims=True) * (1.0 / dm)

        @pl.when(i % tps == 0)
        def _():
            st_ref[...] = jnp.zeros_like(st_ref)

        @pl.when(i == 0)
        def _():
            loss_ref[...] = jnp.zeros_like(loss_ref)

        st_ref[0, 0:1, :] += dgate
        loss_ref[...] += jnp.broadcast_to(part, loss_ref.shape)

    return _pcall(body, name=name, grid=(T // tm,),
                  in_specs=[_row_spec(tm, k), pl.BlockSpec((k, dm), lambda i: (0, 0)), row, row, mod_spec],
                  out_specs=[row, row, stat_spec, loss_spec],
                  out_shape=[_sds((T, dm), F32), _sds((T, dm), MX), _sds((mod3.shape[0], SUBLANE, dm), F32),
                             _sds((SUBLANE, LANE), F32)])(a, w, x2, tgt, mod3)


def _norm_bwd_tail(dhv, x_ref, dxi_ref, f_ref, mod_ref, n_ref, dx_ref, df_ref, st_ref, *, first, sub, coef):
    x = x_ref[...]
    r = _rsq(x)
    xhat = x * r
    n = n_ref[...]
    d_shift = jnp.sum(dhv, axis=0, keepdims=True)
    d_scale = jnp.sum(dhv * (xhat * n), axis=0, keepdims=True)
    dxn = dhv * (1.0 + mod_ref[0, 3 * sub + 1:3 * sub + 2, :])
    d_gain = jnp.sum(dxn * xhat, axis=0, keepdims=True)
    dxhat = dxn * n
    dx = dxi_ref[...] + r * (dxhat - xhat * jnp.mean(dxhat * xhat, axis=-1, keepdims=True))
    dx_ref[...] = dx

    @pl.when(first)
    def _():
        st_ref[...] = jnp.zeros_like(st_ref)

    st_ref[0, 0:1, :] += d_shift
    st_ref[0, 1:2, :] += d_scale
    st_ref[0, 2:3, :] += d_gain
    if f_ref is not None:
        st_ref[0, 3:4, :] += jnp.sum(coef * dx * f_ref[...], axis=0, keepdims=True)
        df_ref[...] = (coef * mod_ref[0, 3 * sub - 1:3 * sub, :] * dx).astype(df_ref.dtype)


def _ffn_dh_norm(dgu, w_st, x_cur, dx_in, f_prev, mod3, gain, *, sub, coef, name, tm=512, comm=None):
    T = dgu.shape[0]
    _, dm, hw = w_st.shape
    tm = _pick(T, tm)
    nb = mod3.shape[0]
    tps = (T // nb) // tm

    def body(d_ref, w_ref, x_ref, dxi_ref, f_ref, mod_ref, n_ref, dx_ref, df_ref, st_ref, acc_ref):
        i = pl.program_id(0)
        q = pl.program_id(1)
        part = lax.dot_general(d_ref[...], w_ref[0], (((1,), (1,)), ((), ())), preferred_element_type=F32)

        @pl.when(q == 0)
        def _():
            acc_ref[...] = part

        @pl.when(q > 0)
        def _():
            acc_ref[...] += part

        @pl.when(q == 3)
        def _():
            _norm_bwd_tail(acc_ref[...], x_ref, dxi_ref, f_ref, mod_ref, n_ref, dx_ref, df_ref, st_ref,
                           first=i % tps == 0, sub=sub, coef=coef)

    row = pl.BlockSpec((tm, dm), lambda i, q: (i, 0))
    return _pcall(body, name=name, grid=(T // tm, 4),
                  in_specs=[pl.BlockSpec((tm, hw), lambda i, q: (i, q)), pl.BlockSpec((1, dm, hw), lambda i, q: (_gu_shard(q), 0, 0)),
                            row, row, row, pl.BlockSpec((1, 9, dm), lambda i, q: (i // tps, 0, 0)),
                            pl.BlockSpec((1, dm), lambda i, q: (0, 0))],
                  out_specs=[row, row, pl.BlockSpec((1, SUBLANE, dm), lambda i, q: (i // tps, 0, 0))],
                  out_shape=[_sds((T, dm), F32), _sds((T, dm), MX), _sds((nb, SUBLANE, dm), F32)],
                  scratch=[pltpu.VMEM((tm, dm), F32)], comm=comm)(dgu, w_st, x_cur, dx_in, f_prev, mod3, gain)


def _mm_nt_norm(a, b, x_cur, dx_in, f_prev, mod3, gain, *, sub, coef, name, tm=512, comm=None):
    T, k = a.shape
    dm = b.shape[0]
    tm = _pick(T, tm)
    nb = mod3.shape[0]
    tps = (T // nb) // tm

    def body(a_ref, b_ref, x_ref, dxi_ref, f_ref, mod_ref, n_ref, dx_ref, df_ref, st_ref):
        dh = lax.dot_general(a_ref[...], b_ref[...], (((1,), (1,)), ((), ())), preferred_element_type=F32)
        _norm_bwd_tail(dh, x_ref, dxi_ref, f_ref, mod_ref, n_ref, dx_ref, df_ref, st_ref,
                       first=pl.program_id(0) % tps == 0, sub=sub, coef=coef)

    row = _row_spec(tm, dm)
    return _pcall(body, name=name, grid=(T // tm,),
                  in_specs=[_row_spec(tm, k), pl.BlockSpec((dm, k), lambda i: (0, 0)), row, row, row,
                            pl.BlockSpec((1, 9, dm), lambda i: (i // tps, 0, 0)), pl.BlockSpec((1, dm), lambda i: (0, 0))],
                  out_specs=[row, row, pl.BlockSpec((1, SUBLANE, dm), lambda i: (i // tps, 0, 0))],
                  out_shape=[_sds((T, dm), F32), _sds((T, dm), MX), _sds((nb, SUBLANE, dm), F32)],
                  comm=comm)(a, b, x_cur, dx_in, f_prev, mod3, gain)


def _bwd_block(x_cur, dh, dx_in, f_prev, mod3, gain, *, sub, coef, name, tm=256, comm=None):
    T, dm = x_cur.shape
    nb = mod3.shape[0]
    tps = (T // nb) // tm
    has_f = f_prev is not None
    mod_spec = pl.BlockSpec((1, 9, dm), lambda i: (i // tps, 0, 0))
    vec_spec = pl.BlockSpec((1, dm), lambda i: (0, 0))
    stat_spec = pl.BlockSpec((1, SUBLANE, dm), lambda i: (i // tps, 0, 0))
    row = _row_spec(tm, dm)

    def body(*refs):
        if has_f:
            x_ref, dh_ref, dxi_ref, f_ref, mod_ref, n_ref, dx_ref, df_ref, st_ref = refs
        else:
            x_ref, dh_ref, dxi_ref, mod_ref, n_ref, dx_ref, st_ref = refs
            f_ref = df_ref = None
        _norm_bwd_tail(dh_ref[...], x_ref, dxi_ref, f_ref, mod_ref, n_ref, dx_ref, df_ref, st_ref,
                       first=pl.program_id(0) % tps == 0, sub=sub, coef=coef)

    st_shape = _sds((nb, SUBLANE, dm), F32)
    if has_f:
        return _pcall(body, name=name, grid=(T // tm,), in_specs=[row, row, row, row, mod_spec, vec_spec],
                      out_specs=[row, row, stat_spec],
                      out_shape=[_sds((T, dm), F32), _sds((T, dm), MX), st_shape], comm=comm)(x_cur, dh, dx_in, f_prev, mod3, gain)
    return _pcall(body, name=name, grid=(T // tm,), in_specs=[row, row, row, mod_spec, vec_spec],
                  out_specs=[row, stat_spec], out_shape=[_sds((T, dm), F32), st_shape], comm=comm)(x_cur, dh, dx_in, mod3, gain)


def _merge_fwd(z, br_pool, br_mla, *, name, tm=256):
    T = z.shape[0]

    def body(z_ref, bp_ref, bm_ref, o_ref):
        gp = z_ref[:, Z_GP:Z_GP + D]
        gm = z_ref[:, Z_GM:Z_GM + D]
        o_ref[...] = (_sigmoid(gp) * bp_ref[...] + _sigmoid(gm) * bm_ref[...]).astype(o_ref.dtype)

    return _pcall(body, name=name, grid=(T // tm,), in_specs=[_row_spec(tm, Z_W), _row_spec(tm, D), _row_spec(tm, D)],
                  out_specs=_row_spec(tm, D), out_shape=_sds((T, D), MX))(z, br_pool, br_mla)


def _merge_bwd(z, br_pool, br_mla, dmerged, *, name, tm=256):
    T = z.shape[0]

    def body(z_ref, bp_ref, bm_ref, dm_ref, dbp_ref, dbm_ref, dg_ref):
        dm = dm_ref[...].astype(F32)
        sp = _sigmoid(z_ref[:, Z_GP:Z_GP + D])
        sm = _sigmoid(z_ref[:, Z_GM:Z_GM + D])
        dbp_ref[...] = (dm * sp).astype(dbp_ref.dtype)
        dbm_ref[...] = (dm * sm).astype(dbm_ref.dtype)
        dg_ref[:, :D] = (dm * bp_ref[...].astype(F32) * sp * (1.0 - sp)).astype(dg_ref.dtype)
        dg_ref[:, D:] = (dm * bm_ref[...].astype(F32) * sm * (1.0 - sm)).astype(dg_ref.dtype)

    return _pcall(body, name=name, grid=(T // tm,),
                  in_specs=[_row_spec(tm, Z_W), _row_spec(tm, D), _row_spec(tm, D), _row_spec(tm, D)],
                  out_specs=[_row_spec(tm, D), _row_spec(tm, D), _row_spec(tm, 2 * D)],
                  out_shape=[_sds((T, D), MX), _sds((T, D), MX), _sds((T, 2 * D), MX)])(z, br_pool, br_mla, dmerged)


def _shift_down(x, k, row):
    return jnp.where(row >= k, pltpu.roll(x, k, 0), 0.0)


def _shift_up(x, k, row, n):
    return jnp.where(row < n - k, pltpu.roll(x, n - k, 0), 0.0)


def _pool_fwd(z, pool_grp, pool_scale, *, nb, name):
    T = z.shape[0]
    S = T // nb
    blk = pl.BlockSpec((S, LANE), lambda b, g: (b, g))

    def body(u_ref, w_ref, s_ref, pooled_ref, mixed_ref, scaled_ref):
        g = pl.program_id(1)
        u = u_ref[...]
        row = lax.broadcasted_iota(jnp.int32, u.shape, 0)
        s2 = u + _shift_down(u, 1, row)
        s4 = s2 + _shift_down(s2, 2, row)
        s8 = s4 + _shift_down(s4, 4, row)
        s16 = s8 + _shift_down(s8, 8, row)
        win = jnp.where(g == 0, s2, jnp.where(g == 1, s4, jnp.where(g == 2, s8, s16)))
        width = lax.shift_left(jnp.int32(2), g)
        cnt = jnp.minimum(row + 1, width).astype(F32)
        pooled = (win / cnt - u).astype(MX)
        pooled_ref[...] = pooled
        mixed = jnp.dot(pooled, w_ref[0], preferred_element_type=F32)
        mixed_ref[...] = mixed
        scaled_ref[...] = (mixed * s_ref[...]).astype(scaled_ref.dtype)

    return _pcall(body, name=name, grid=(nb, POOL_G),
                  in_specs=[blk, pl.BlockSpec((1, LANE, LANE), lambda b, g: (g, 0, 0)),
                            pl.BlockSpec((1, LANE), lambda b, g: (0, g))],
                  out_specs=[blk, blk, blk],
                  out_shape=[_sds((T, POOL_W), MX), _sds((T, POOL_W), F32), _sds((T, POOL_W), MX)])(z, pool_grp, pool_scale)


def _pool_bwd(dscaled, mixed, pooled, pool_grp, pool_scale, *, nb, name):
    T = dscaled.shape[0]
    S = T // nb
    blk = pl.BlockSpec((S, LANE), lambda g, b: (b, g))

    def body(ds_ref, mixed_ref, pooled_ref, w_ref, s_ref, du_ref, dw_ref, dsc_ref):
        g = pl.program_id(0)
        b = pl.program_id(1)
        ds = ds_ref[...]
        dsc_ref[0] = jnp.sum(ds * mixed_ref[...], axis=0, keepdims=True)
        dmixed = (ds * s_ref[...]).astype(MX)
        dw = lax.dot_general(pooled_ref[...], dmixed, (((0,), (0,)), ((), ())), preferred_element_type=F32)

        @pl.when(b == 0)
        def _():
            dw_ref[0] = dw

        @pl.when(b > 0)
        def _():
            dw_ref[0] += dw
        dpooled = lax.dot_general(dmixed, w_ref[0], (((1,), (1,)), ((), ())), preferred_element_type=F32)
        row = lax.broadcasted_iota(jnp.int32, dpooled.shape, 0)
        width = lax.shift_left(jnp.int32(2), g)
        q = dpooled / jnp.minimum(row + 1, width).astype(F32)
        r2 = q + _shift_up(q, 1, row, S)
        r4 = r2 + _shift_up(r2, 2, row, S)
        r8 = r4 + _shift_up(r4, 4, row, S)
        r16 = r8 + _shift_up(r8, 8, row, S)
        win = jnp.where(g == 0, r2, jnp.where(g == 1, r4, jnp.where(g == 2, r8, r16)))
        du_ref[...] = (win - dpooled).astype(du_ref.dtype)

    return _pcall(body, name=name, grid=(POOL_G, nb),
                  in_specs=[blk, blk, blk, pl.BlockSpec((1, LANE, LANE), lambda g, b: (g, 0, 0)),
                            pl.BlockSpec((1, LANE), lambda g, b: (0, g))],
                  out_specs=[blk, pl.BlockSpec((1, LANE, LANE), lambda g, b: (g, 0, 0)),
                             pl.BlockSpec((1, 1, LANE), lambda g, b: (b, 0, g))],
                  out_shape=[_sds((T, POOL_W), MX), _sds((POOL_G, LANE, LANE), F32), _sds((nb, 1, POOL_W), F32)],
                  )(dscaled, mixed, pooled, pool_grp, pool_scale)


def _lane_masks(shape):
    lane = lax.broadcasted_iota(jnp.int32, shape, len(shape) - 1)
    m_n = lane < NOPE
    m_r = jnp.logical_and(lane >= KR_LANE, lane < KR_LANE + ROPE)
    first_half = lane < KR_LANE + ROPE // 2
    return m_n, m_r, first_half


def _rot(y, first_half):
    return jnp.where(first_half, -pltpu.roll(y, LANE - ROPE // 2, 1), pltpu.roll(y, ROPE // 2, 1))


def _rot_t(v, first_half, m_r):
    return jnp.where(m_r, jnp.where(first_half, pltpu.roll(v, LANE - ROPE // 2, 1), -pltpu.roll(v, ROPE // 2, 1)), 0.0)


def _mla_in_fwd(z, w_q, w_kv, qa_gain, kva_gain, cos, sin, q_gain, k_gain, *, name, tm=256):
    T = z.shape[0]
    slab = pl.BlockSpec((tm, LANE), lambda i: (i, 0))
    vec = pl.BlockSpec((1, LANE), lambda i: (0, 0))
    whole = lambda a: pl.BlockSpec(a.shape, lambda i: (0, 0))

    def body(z_ref, wq_ref, wkv_ref, qa_ref, kva_ref, cos_ref, sin_ref, qg_ref, kg_ref,
             qn_ref, kvn_ref, qp_ref, kvp_ref, q_ref, k_ref, v_ref):
        ql = z_ref[:, Z_QL:Z_QL + QL]
        kvl = z_ref[:, Z_KV:Z_KV + KVL]
        qn = (ql * _rsq(ql) * qa_ref[...]).astype(MX)
        kvn = (kvl * _rsq(kvl) * kva_ref[...]).astype(MX)
        qn_ref[...] = qn
        kvn_ref[...] = kvn
        qp = jnp.dot(qn, wq_ref[...], preferred_element_type=F32)
        kvp = jnp.dot(kvn, wkv_ref[...], preferred_element_type=F32)
        qp_ref[...] = qp
        kvp_ref[...] = kvp
        m_n, m_r, first_half = _lane_masks((tm, LANE))
        c = cos_ref[...]
        s = sin_ref[...]
        qg = qg_ref[...]
        kg = kg_ref[...]
        xr = z_ref[:, Z_KR:Z_KR + LANE]
        rr = lax.rsqrt(jnp.sum(xr * xr, axis=-1, keepdims=True) * (1.0 / ROPE) + EPS)
        yr = xr * rr * kg
        kr = jnp.where(m_r, yr * c + _rot(yr, first_half) * s, 0.0)
        for h in range(NH):
            x = qp[:, h * LANE:(h + 1) * LANE]
            x2 = x * x
            rn = lax.rsqrt(jnp.sum(jnp.where(m_n, x2, 0.0), axis=-1, keepdims=True) * (1.0 / NOPE) + EPS)
            rq = lax.rsqrt(jnp.sum(jnp.where(m_r, x2, 0.0), axis=-1, keepdims=True) * (1.0 / ROPE) + EPS)
            y = x * jnp.where(m_n, rn, jnp.where(m_r, rq, 0.0)) * qg
            q_ref[:, h * LANE:(h + 1) * LANE] = (y * c + _rot(y, first_half) * s).astype(q_ref.dtype)
            xk = kvp[:, h * LANE:(h + 1) * LANE]
            rk = lax.rsqrt(jnp.sum(jnp.where(m_n, xk * xk, 0.0), axis=-1, keepdims=True) * (1.0 / NOPE) + EPS)
            k_ref[:, h * LANE:(h + 1) * LANE] = (jnp.where(m_n, xk * rk * kg, 0.0) + kr).astype(k_ref.dtype)
        v_ref[...] = kvp[:, NH * LANE:].astype(v_ref.dtype)

    return _pcall(body, name=name, grid=(T // tm,),
                  in_specs=[_row_spec(tm, LAT_W), whole(w_q), whole(w_kv), whole(qa_gain), whole(kva_gain), slab, slab, vec, vec],
                  out_specs=[_row_spec(tm, QL), _row_spec(tm, KVL), _row_spec(tm, NH * LANE), _row_spec(tm, NH * LANE + NH * NOPE),
                             _row_spec(tm, NH * LANE), _row_spec(tm, NH * LANE), _row_spec(tm, NH * NOPE)],
                  out_shape=[_sds((T, QL), MX), _sds((T, KVL), MX), _sds((T, NH * LANE), F32), _sds((T, NH * LANE + NH * NOPE), F32),
                             _sds((T, NH * LANE), MX), _sds((T, NH * LANE), MX), _sds((T, NH * NOPE), MX)],
                  )(z, w_q, w_kv, qa_gain, kva_gain, cos, sin, q_gain, k_gain)


def _mla_in_bwd(dq, dk, dv, qp, kvp, z, w_q, w_kv, qa_gain, kva_gain, cos, sin, q_gain, k_gain, *, nb, name, tm=256):
    T = qp.shape[0]
    tps = (T // nb) // tm
    slab = pl.BlockSpec((tm, LANE), lambda i: (i, 0))
    vec = pl.BlockSpec((1, LANE), lambda i: (0, 0))
    whole = lambda a: pl.BlockSpec(a.shape, lambda i: (0, 0))

    def latent_bwd(x, dy, gain):
        r = _rsq(x)
        xhat = x * r
        dxhat = dy * gain
        return r * (dxhat - xhat * jnp.mean(dxhat * xhat, axis=-1, keepdims=True)), jnp.sum(dy * xhat, axis=0, keepdims=True)

    def body(dq_ref, dk_ref, dv_ref, qp_ref, kvp_ref, z_ref, wq_ref, wkv_ref, qa_ref, kva_ref, cos_ref, sin_ref, qg_ref, kg_ref,
             dqp_ref, dkvp_ref, dkr_ref, dql_ref, dkvl_ref, st_ref, sq_ref, sk_ref):
        i = pl.program_id(0)
        m_n, m_r, first_half = _lane_masks((tm, LANE))
        c = cos_ref[...]
        s = sin_ref[...]
        qg = qg_ref[...]
        kg = kg_ref[...]
        dqg = jnp.zeros((1, LANE), F32)
        dkg = jnp.zeros((1, LANE), F32)
        dkr_sum = jnp.zeros((tm, LANE), F32)
        for h in range(NH):
            x = qp_ref[:, h * LANE:(h + 1) * LANE]
            x2 = x * x
            rn = lax.rsqrt(jnp.sum(jnp.where(m_n, x2, 0.0), axis=-1, keepdims=True) * (1.0 / NOPE) + EPS)
            rq = lax.rsqrt(jnp.sum(jnp.where(m_r, x2, 0.0), axis=-1, keepdims=True) * (1.0 / ROPE) + EPS)
            rfac = jnp.where(m_n, rn, jnp.where(m_r, rq, 0.0))
            xhat = x * rfac
            do = dq_ref[:, h * LANE:(h + 1) * LANE]
            dy = do * c + _rot_t(do * s, first_half, m_r)
            dqg = dqg + jnp.sum(dy * xhat, axis=0, keepdims=True)
            dxhat = dy * qg
            t = dxhat * xhat
            mean_n = jnp.sum(jnp.where(m_n, t, 0.0), axis=-1, keepdims=True) * (1.0 / NOPE)
            mean_r = jnp.sum(jnp.where(m_r, t, 0.0), axis=-1, keepdims=True) * (1.0 / ROPE)
            dqp_ref[:, h * LANE:(h + 1) * LANE] = (
                rfac * (dxhat - xhat * jnp.where(m_n, mean_n, jnp.where(m_r, mean_r, 0.0)))).astype(dqp_ref.dtype)

            xk = kvp_ref[:, h * LANE:(h + 1) * LANE]
            rk = lax.rsqrt(jnp.sum(jnp.where(m_n, xk * xk, 0.0), axis=-1, keepdims=True) * (1.0 / NOPE) + EPS)
            khat = jnp.where(m_n, xk * rk, 0.0)
            dko = dk_ref[:, h * LANE:(h + 1) * LANE]
            dkn = jnp.where(m_n, dko, 0.0)
            dkg = dkg + jnp.sum(dkn * khat, axis=0, keepdims=True)
            dkhat = dkn * kg
            mean_k = jnp.sum(dkhat * khat, axis=-1, keepdims=True) * (1.0 / NOPE)
            dkvp_ref[:, h * LANE:(h + 1) * LANE] = jnp.where(m_n, rk * (dkhat - khat * mean_k), 0.0).astype(dkvp_ref.dtype)
            dkr_sum = dkr_sum + jnp.where(m_r, dko, 0.0)
        dkvp_ref[:, NH * LANE:] = dv_ref[...].astype(dkvp_ref.dtype)

        xr = z_ref[:, Z_KR:Z_KR + LANE]
        rr = lax.rsqrt(jnp.sum(xr * xr, axis=-1, keepdims=True) * (1.0 / ROPE) + EPS)
        rhat = xr * rr
        dyr = dkr_sum * c + _rot_t(dkr_sum * s, first_half, m_r)
        dkg = dkg + jnp.sum(dyr * rhat, axis=0, keepdims=True)
        drhat = dyr * kg
        mean_kr = jnp.sum(drhat * rhat, axis=-1, keepdims=True) * (1.0 / ROPE)
        dkr_ref[...] = jnp.where(m_r, rr * (drhat - rhat * mean_kr), 0.0).astype(dkr_ref.dtype)

        nt = (((1,), (1,)), ((), ()))
        dqn = lax.dot_general(dqp_ref[...], wq_ref[...], nt, preferred_element_type=F32)
        dkvn = lax.dot_general(dkvp_ref[...], wkv_ref[...], nt, preferred_element_type=F32)
        dql, dqa = latent_bwd(z_ref[:, Z_QL:Z_QL + QL], dqn, qa_ref[...])
        dkvl, dkva = latent_bwd(z_ref[:, Z_KV:Z_KV + KVL], dkvn, kva_ref[...])
        dql_ref[...] = dql.astype(dql_ref.dtype)
        dkvl_ref[...] = dkvl.astype(dkvl_ref.dtype)

        @pl.when(i % tps == 0)
        def _():
            st_ref[...] = jnp.zeros_like(st_ref)
            sq_ref[...] = jnp.zeros_like(sq_ref)
            sk_ref[...] = jnp.zeros_like(sk_ref)

        st_ref[0, 0:1, :] += dqg
        st_ref[0, 1:2, :] += dkg
        sq_ref[0, 0:1, :] += dqa
        sk_ref[0, 0:1, :] += dkva

    stat = lambda w: pl.BlockSpec((1, SUBLANE, w), lambda i: (i // tps, 0, 0))
    return _pcall(body, name=name, grid=(T // tm,),
                  in_specs=[_row_spec(tm, NH * LANE), _row_spec(tm, NH * LANE), _row_spec(tm, NH * NOPE),
                            _row_spec(tm, NH * LANE), _row_spec(tm, NH * LANE + NH * NOPE), _row_spec(tm, LAT_W),
                            whole(w_q), whole(w_kv), whole(qa_gain), whole(kva_gain), slab, slab, vec, vec],
                  out_specs=[_row_spec(tm, NH * LANE), _row_spec(tm, NH * LANE + NH * NOPE), slab, _row_spec(tm, QL),
                             _row_spec(tm, KVL), stat(LANE), stat(QL), stat(KVL)],
                  out_shape=[_sds((T, NH * LANE), MX), _sds((T, NH * LANE + NH * NOPE), MX), _sds((T, LANE), MX),
                             _sds((T, QL), MX), _sds((T, KVL), MX), _sds((nb, SUBLANE, LANE), F32),
                             _sds((nb, SUBLANE, QL), F32), _sds((nb, SUBLANE, KVL), F32)],
                  )(dq, dk, dv, qp, kvp, z, w_q, w_kv, qa_gain, kva_gain, cos, sin, q_gain, k_gain)


def _lower_triangle(t):
    return lax.broadcasted_iota(jnp.int32, (t, t), 1) <= lax.broadcasted_iota(jnp.int32, (t, t), 0)


def _attn_fwd(q, k, v, *, nb, name, tq=512, comm=None):
    T = q.shape[0]
    S = T // nb
    tq = _pick(S, tq)
    nq = S // tq
    tk = tq
    npair = NH // 2

    def body(q_ref, k_ref, v_ref, o_ref, lse_ref):
        qi = pl.program_id(2)
        lane = lax.broadcasted_iota(jnp.int32, (tq, LANE), 1)
        qs = [q_ref[:, hh * LANE:(hh + 1) * LANE] for hh in range(2)]

        def block(j, carry, diagonal):
            k0 = pl.multiple_of(j * tk, tk)
            vb = v_ref[pl.ds(k0, tk), :]
            new = []
            for hh in range(2):
                m, l, acc = carry[hh]
                kb = k_ref[pl.ds(k0, tk), hh * LANE:(hh + 1) * LANE]
                s = lax.dot_general(qs[hh], kb, (((1,), (1,)), ((), ())), preferred_element_type=F32) * ATTN_SCALE
                if diagonal:
                    s = jnp.where(_lower_triangle(tq), s, NEG)
                m_new = jnp.maximum(m, jnp.max(s, axis=-1, keepdims=True))
                p = jnp.exp(s - m_new)
                alpha = jnp.exp(m - m_new)
                l = alpha * l + jnp.sum(p, axis=-1, keepdims=True)
                acc = alpha * acc + jnp.dot(p.astype(MX), vb, preferred_element_type=F32)
                new.append((m_new, l, acc))
            return tuple(new)

        init = tuple((jnp.full((tq, 1), NEG, F32), jnp.zeros((tq, 1), F32), jnp.zeros((tq, LANE), F32)) for _ in range(2))
        carry = lax.fori_loop(0, qi, lambda j, c: block(j, c, False), init)
        (m0, l0, acc0), (m1, l1, acc1) = block(qi, carry, True)
        o_ref[...] = jnp.where(lane < NOPE, acc0 / l0, acc1 / l1).astype(o_ref.dtype)
        lse_ref[...] = jnp.where(lane < NOPE, m0 + jnp.log(l0), m1 + jnp.log(l1))

    return _pcall(body, name=name, grid=(nb, npair, nq),
                  in_specs=[pl.BlockSpec((tq, 2 * LANE), lambda b, p, i: (b * nq + i, p)),
                            pl.BlockSpec((S, 2 * LANE), lambda b, p, i: (b, p)),
                            pl.BlockSpec((S, LANE), lambda b, p, i: (b, p))],
                  out_specs=[pl.BlockSpec((tq, LANE), lambda b, p, i: (b * nq + i, p)),
                             pl.BlockSpec((tq, LANE), lambda b, p, i: (b * nq + i, p))],
                  out_shape=[_sds((T, NH * NOPE), MX), _sds((T, NH * NOPE), F32)], comm=comm)(q, k, v)


def _attn_bwd(q, k, v, o, lse, do, *, nb, name, tq=512, comm=None):
    T = q.shape[0]
    S = T // nb
    tq = _pick(S, tq)
    nq = S // tq
    tk = tq
    npair = NH // 2

    def body(q_ref, k_ref, v_ref, o_ref, lse_ref, do_ref, dq_ref, dk_ref, dv_ref, delta_ref):
        lane = lax.broadcasted_iota(jnp.int32, (tq, LANE), 1)
        first = lane < NOPE
        dq_ref[...] = jnp.zeros_like(dq_ref)

        def delta_step(qi, _):
            q0 = pl.multiple_of(qi * tq, tq)
            prod = do_ref[pl.ds(q0, tq), :] * o_ref[pl.ds(q0, tq), :].astype(F32)
            d0 = jnp.sum(jnp.where(first, prod, 0.0), axis=-1, keepdims=True)
            d1 = jnp.sum(jnp.where(first, 0.0, prod), axis=-1, keepdims=True)
            delta_ref[pl.ds(q0, tq), :] = jnp.where(first, d0, d1)
            return 0

        lax.fori_loop(0, nq, delta_step, 0)

        def kv_step(kj, _):
            k0 = pl.multiple_of(kj * tk, tk)
            kbs = [k_ref[pl.ds(k0, tk), hh * LANE:(hh + 1) * LANE] for hh in range(2)]
            vb = v_ref[pl.ds(k0, tk), :]

            def q_block(qi, carry, diagonal):
                dk0, dk1, dv = carry
                dks = [dk0, dk1]
                q0 = pl.multiple_of(qi * tq, tq)
                dov = do_ref[pl.ds(q0, tq), :]
                lse_v = lse_ref[pl.ds(q0, tq), :]
                delta_v = delta_ref[pl.ds(q0, tq), :]
                for hh in range(2):
                    qh = q_ref[pl.ds(q0, tq), hh * LANE:(hh + 1) * LANE]
                    dob = jnp.where(first if hh == 0 else jnp.logical_not(first), dov, 0.0).astype(MX)
                    s = lax.dot_general(qh, kbs[hh], (((1,), (1,)), ((), ())), preferred_element_type=F32) * ATTN_SCALE
                    p = jnp.exp(s - lse_v[:, hh * NOPE:hh * NOPE + 1])
                    if diagonal:
                        p = jnp.where(_lower_triangle(tq), p, 0.0)
                    dp = lax.dot_general(dob, vb, (((1,), (1,)), ((), ())), preferred_element_type=F32)
                    ds = (p * (dp - delta_v[:, hh * NOPE:hh * NOPE + 1]) * ATTN_SCALE).astype(MX)
                    dks[hh] = dks[hh] + lax.dot_general(ds, qh, (((0,), (0,)), ((), ())), preferred_element_type=F32)
                    dv = dv + lax.dot_general(p.astype(MX), dob, (((0,), (0,)), ((), ())), preferred_element_type=F32)
                    dq_ref[pl.ds(q0, tq), hh * LANE:(hh + 1) * LANE] += jnp.dot(ds, kbs[hh], preferred_element_type=F32)
                return dks[0], dks[1], dv

            zero = jnp.zeros((tk, LANE), F32)
            carry = q_block(kj, (zero, zero, zero), True)
            dk0, dk1, dv = lax.fori_loop(kj + 1, nq, lambda qi, c: q_block(qi, c, False), carry)
            dk_ref[pl.ds(k0, tk), 0:LANE] = dk0
            dk_ref[pl.ds(k0, tk), LANE:2 * LANE] = dk1
            dv_ref[pl.ds(k0, tk), :] = dv
            return 0

        lax.fori_loop(0, nq, kv_step, 0)

    pair256 = pl.BlockSpec((S, 2 * LANE), lambda b, p: (b, p))
    pair128 = pl.BlockSpec((S, LANE), lambda b, p: (b, p))
    return _pcall(body, name=name, grid=(nb, npair),
                  in_specs=[pair256, pair256, pair128, pair128, pair128, pair128],
                  out_specs=[pair256, pair256, pair128],
                  out_shape=[_sds((T, NH * LANE), F32), _sds((T, NH * LANE), F32), _sds((T, NH * NOPE), F32)],
                  scratch=[pltpu.VMEM((S, LANE), F32)], comm=comm)(q, k, v, o, lse, do)


def _run(plan, fn, *args, name, **kw):
    rider = plan.rider(name)
    if rider is None:
        return fn(*args, name=name, **kw)
    outs, landed = fn(*args, name=name, comm=rider, **kw)
    plan.landed(name, landed)
    return outs


def _layer_fwd_bwd(x, tgt, mod3, cos, sin, plan, P):
    nb = mod3.shape[0]
    W = plan.W
    h1, gu1, a1 = _run(plan, _ffn_up_first, x, mod3, P["norm_ffn1"], W["ffn1_in"], sub=0, name="ffn1_up_a")
    gu1, a1 = _run(plan, _ffn_up_second, h1, W["ffn1_in"], gu1, a1, name="ffn1_up_b")
    f1, x1, h2 = _run(plan, _mm_resid_norm, a1, W["ffn1_out"], x, mod3, P["norm_mix"], sub=1, coef=0.5, name="ffn1_out")
    z = _run(plan, _mm, h2, W["w_in"], mode="nn", out_dtype=F32, name="mix_in", tn=1664)
    pooled, mixed, scaled = _pool_fwd(z, W["pool_grp"], P["pool_scale"], nb=nb, name="pool_fwd")
    br_pool = _mm(scaled, W["pool_proj"], mode="nn", out_dtype=MX, name="pool_proj")
    qn, kvn, qp, kvp, q, k, v = _mla_in_fwd(z, W["q_up"], W["kv_up"], P["q_a_norm"], P["kv_a_norm"], cos, sin,
                                            P["q_gain"], P["k_gain"], name="mla_in_fwd")
    attn, lse = _run(plan, _attn_fwd, q, k, v, nb=nb, name="attn_fwd")
    br_mla = _mm(attn, W["mla_proj"], mode="nn", out_dtype=MX, name="mla_proj")
    merged = _merge_fwd(z, br_pool, br_mla, name="merge_fwd")
    mo, x2, h3 = _mm_resid_norm(merged, W["w_out"], x1, mod3, P["norm_ffn2"], sub=2, coef=1.0, name="mix_out")
    gu2, a2 = _ffn_up(h3, W["ffn2_in"], name="ffn2_up")
    dy, df2, st_fin, loss = _mm_loss(a2, W["ffn2_out"], x2, tgt, mod3, name="ffn2_out_loss")

    plan.grad("ffn2_out", _dw(a2, df2, name="d_ffn2_out"))
    dgu2 = _ffn_dact(df2, gu2, W["ffn2_out"], name="ffn2_dact")
    plan.grad("ffn2_in", _ffn_dw_in(h3, dgu2, name="d_ffn2_in"))
    dx2, dmo, st3 = _run(plan, _ffn_dh_norm, dgu2, W["ffn2_in"], x2, dy, mo, mod3, P["norm_ffn2"], sub=2, coef=1.0,
                         name="d_ffn2_h")
    plan.grad("w_out", _dw(merged, dmo, name="d_mix_out"))
    dmerged = _mm(dmo, W["w_out"], mode="nt", out_dtype=MX, name="d_merged")
    dbr_pool, dbr_mla, dgates = _merge_bwd(z, br_pool, br_mla, dmerged, name="merge_bwd")
    plan.grad("pool_proj", _dw(scaled, dbr_pool, name="d_pool_proj"))
    dscaled = _mm(dbr_pool, W["pool_proj"], mode="nt", out_dtype=F32, name="d_pool_scaled")
    du_pool, d_pool_grp, d_pool_scale = _pool_bwd(dscaled, mixed, pooled, W["pool_grp"], P["pool_scale"], nb=nb, name="pool_bwd")
    plan.grad("mla_proj", _dw(attn, dbr_mla, name="d_mla_proj"))
    dattn = _mm(dbr_mla, W["mla_proj"], mode="nt", out_dtype=F32, name="d_attn")
    dq, dk, dv = _run(plan, _attn_bwd, q, k, v, attn, lse, dattn, nb=nb, name="attn_bwd")
    dqp, dkvp, dkr, dql, dkvl, st_prep, st_q, st_kv = _mla_in_bwd(
        dq, dk, dv, qp, kvp, z, W["q_up"], W["kv_up"], P["q_a_norm"], P["kv_a_norm"], cos, sin, P["q_gain"], P["k_gain"],
        nb=nb, name="mla_in_bwd")
    plan.grad("q_up", _dw(qn, dqp, name="d_q_up"))
    plan.grad("kv_up", _dw(kvn, dkvp, name="d_kv_up"))
    dz = jnp.concatenate([du_pool, dql, dkvl, dkr, dgates], axis=1)
    plan.grad("w_in", _run(plan, _dw, h2, dz, name="d_mix_in", tm=256, tn=1664))
    dx1, df1, st2 = _run(plan, _mm_nt_norm, dz, W["w_in"], x1, dx2, f1, mod3, P["norm_mix"], sub=1, coef=0.5, name="d_mix_h")
    plan.grad("ffn1_out", _run(plan, _dw, a1, df1, name="d_ffn1_out"))
    dgu1 = _run(plan, _ffn_dact, df1, gu1, W["ffn1_out"], name="ffn1_dact")
    half = x.shape[1] // 2
    plan.grad("ffn1_in@0", _run(plan, _ffn_dw_in, h1, dgu1, rows=(0, half), name="d_ffn1_in_a"))
    plan.grad("ffn1_in@1", _run(plan, _ffn_dw_in, h1, dgu1, rows=(half, half), name="d_ffn1_in_b"))
    dh1 = _run(plan, _ffn_dh, dgu1, W["ffn1_in"], name="d_ffn1_h")
    grad_x, st1 = _run(plan, _bwd_block, x, dh1, dx1, None, mod3, P["norm_ffn1"], sub=0, coef=0.0, name="bwd_norm1")

    return loss, grad_x, (st1, st2, st3, st_fin, st_q, st_kv, st_prep, d_pool_scale), d_pool_grp


def _w_in_to_kernel(w):
    k = w.shape[0]
    zeros = lambda n: jnp.zeros((k, n), w.dtype)
    return jnp.concatenate([w[:, 0:1152], zeros(KR_LANE), w[:, 1152:1184], zeros(LANE - KR_LANE - ROPE), w[:, 1184:]], axis=1)


def _w_in_from_kernel(g):
    return jnp.concatenate([g[:, 0:1152], g[:, Z_KR + KR_LANE:Z_KR + KR_LANE + ROPE], g[:, Z_GP:]], axis=1)


def _q_up_to_kernel(w):
    k = w.shape[0]
    return jnp.pad(w.reshape(k, NH, NOPE + ROPE), ((0, 0), (0, 0), (0, LANE - NOPE - ROPE))).reshape(k, NH * LANE)


def _q_up_from_kernel(g):
    k = g.shape[0]
    return g.reshape(k, NH, LANE)[:, :, :NOPE + ROPE].reshape(k, NH * (NOPE + ROPE))


def _kv_up_to_kernel(w):
    k = w.shape[0]
    w3 = w.reshape(k, NH, 2 * NOPE)
    kpart = jnp.pad(w3[:, :, :NOPE], ((0, 0), (0, 0), (0, LANE - NOPE))).reshape(k, NH * LANE)
    return jnp.concatenate([kpart, w3[:, :, NOPE:].reshape(k, NH * NOPE)], axis=1)


def _kv_up_from_kernel(g):
    k = g.shape[0]
    kpart = g[:, :NH * LANE].reshape(k, NH, LANE)[:, :, :NOPE]
    vpart = g[:, NH * LANE:].reshape(k, NH, NOPE)
    return jnp.concatenate([kpart, vpart], axis=2).reshape(k, NH * 2 * NOPE)


def _gain_slab(nope, rope):
    return jnp.concatenate([nope, rope, jnp.zeros((1, LANE - NOPE - ROPE), nope.dtype)], axis=1)


def _rope_tables(positions):
    inv_freq = 10000.0 ** (-jnp.arange(0, ROPE, 2, dtype=F32) / ROPE)
    ang = positions.astype(F32)[:, None] * inv_freq
    ang = jnp.concatenate([ang, ang], axis=-1)
    t = positions.shape[0]
    cos = jnp.concatenate([jnp.ones((t, KR_LANE), F32), jnp.cos(ang), jnp.ones((t, LANE - KR_LANE - ROPE), F32)], axis=1)
    sin = jnp.concatenate([jnp.zeros((t, KR_LANE), F32), jnp.sin(ang), jnp.zeros((t, LANE - KR_LANE - ROPE), F32)], axis=1)
    return cos, sin


def _coords():
    return lax.axis_index("x"), lax.axis_index("y"), lax.axis_index("c")


HBM_SPEC = pl.BlockSpec(memory_space=pl.ANY)
VMEM_SPEC = pl.BlockSpec(memory_space=pltpu.VMEM)


CHIP_RELS = ((1, 0), (0, 1), (1, 1))


def _gather_comm(shards):
    n = len(shards)

    def ici(ins, outs, sems, a, j, x, y, cc):
        half = ins[a].shape[0] // 2
        mine = pl.ds(cc * half, half)
        dx, dy = CHIP_RELS[j]
        return pltpu.make_async_remote_copy(src_ref=ins[a].at[mine], dst_ref=outs[a].at[2 * x + y, mine],
                                            send_sem=sems[0].at[a, j], recv_sem=sems[1].at[a, j],
                                            device_id=(x ^ dx, y ^ dy, cc), device_id_type=MESH)

    def d2d(ins, outs, sems, a, j, x, y, cc, half_of):
        half = ins[a].shape[0] // 2
        dx, dy = CHIP_RELS[j]
        landed = outs[a].at[2 * (x ^ dx) + (y ^ dy), pl.ds(half_of * half, half)]
        return pltpu.make_async_remote_copy(src_ref=landed, dst_ref=landed, send_sem=sems[2].at[a, j], recv_sem=sems[3].at[a, j],
                                            device_id=(x, y, 1 - cc), device_id_type=MESH)

    def own(ins, outs, sems, a, x, y):
        return pltpu.make_async_copy(ins[a], outs[a].at[2 * x + y], sems[4].at[a])

    def start(ins, outs, sems):
        x, y, cc = _coords()
        for a in range(n):
            own(ins, outs, sems, a, x, y).start()
            for j in range(3):
                ici(ins, outs, sems, a, j, x, y, cc).start()

    def finish(ins, outs, sems):
        x, y, cc = _coords()
        for a in range(n):
            for j in range(3):
                ici(ins, outs, sems, a, j, x, y, cc).wait_recv()
                d2d(ins, outs, sems, a, j, x, y, cc, cc).start()
        for a in range(n):
            for j in range(3):
                d2d(ins, outs, sems, a, j, x, y, cc, 1 - cc).wait_recv()
        for a in range(n):
            for j in range(3):
                ici(ins, outs, sems, a, j, x, y, cc).wait_send()
                d2d(ins, outs, sems, a, j, x, y, cc, cc).wait_send()
            own(ins, outs, sems, a, x, y).wait()

    dma = pltpu.SemaphoreType.DMA
    return _Comm(shards, [_sds((N_CHIP,) + s.shape, s.dtype) for s in shards],
                 [dma((n, 3)), dma((n, 3)), dma((n, 3)), dma((n, 3)), dma((n,))], start, finish)


def _swap_comm(parts):
    n = len(parts)

    def copy(ins, outs, sems, a):
        x, y, cc = _coords()
        half = ins[a].shape[1] // 2
        return pltpu.make_async_remote_copy(src_ref=ins[a].at[:, pl.ds((1 - cc) * half, half)], dst_ref=outs[a],
                                            send_sem=sems[0].at[a], recv_sem=sems[1].at[a], device_id=(x, y, 1 - cc),
                                            device_id_type=MESH)

    def start(ins, outs, sems):
        for a in range(n):
            copy(ins, outs, sems, a).start()

    def finish(ins, outs, sems):
        for a in range(n):
            copy(ins, outs, sems, a).wait()

    dma = pltpu.SemaphoreType.DMA
    return _Comm(parts, [_sds((p.shape[0], p.shape[1] // 2, p.shape[2]), p.dtype) for p in parts], [dma((n,)), dma((n,))],
                 start, finish)


def _add_half(full, other, cidx, *, name):
    nch, r, c = full.shape
    half = r // 2
    tr = _pick_rows(half)
    nbk = half // tr
    grid_spec = pltpu.PrefetchScalarGridSpec(
        num_scalar_prefetch=1, grid=(nch, nbk),
        in_specs=[pl.BlockSpec((1, tr, c), lambda j, i, cref: (j, cref[0] * nbk + i, 0)),
                  pl.BlockSpec((1, tr, c), lambda j, i, cref: (j, i, 0))],
        out_specs=pl.BlockSpec((1, tr, c), lambda j, i, cref: (j, i, 0)))

    def body(cref, a_ref, b_ref, o_ref):
        o_ref[...] = (a_ref[...] + b_ref[...]).astype(o_ref.dtype)

    return _pcall(body, name=name, out_shape=_sds((nch, half, c), MX), grid_spec=grid_spec)(cidx, full, other)


def _pick_rows(rows, target=512):
    best = None
    for t in range(16, min(rows, target) + 1, 16):
        if rows % t == 0:
            best = t
    return rows if best is None else best


def _exchange_comm(parts):
    n = len(parts)

    def send(ins, outs, sems, a, j, x, y, cc):
        dx, dy = CHIP_RELS[j]
        return pltpu.make_async_remote_copy(src_ref=ins[a].at[2 * (x ^ dx) + (y ^ dy)], dst_ref=outs[a].at[2 * x + y],
                                            send_sem=sems[0].at[a, j], recv_sem=sems[1].at[a, j],
                                            device_id=(x ^ dx, y ^ dy, cc), device_id_type=MESH)

    def landing(ins, outs, sems, a, j, x, y, cc):
        dx, dy = CHIP_RELS[j]
        peer_chip = 2 * (x ^ dx) + (y ^ dy)
        return pltpu.make_async_remote_copy(src_ref=ins[a].at[peer_chip], dst_ref=outs[a].at[peer_chip], send_sem=sems[0].at[a, j],
                                            recv_sem=sems[1].at[a, j], device_id=(x, y, cc), device_id_type=MESH)

    def own(ins, outs, sems, a, x, y):
        return pltpu.make_async_copy(ins[a].at[2 * x + y], outs[a].at[2 * x + y], sems[2].at[a])

    def start(ins, outs, sems):
        x, y, cc = _coords()
        for a in range(n):
            own(ins, outs, sems, a, x, y).start()
            for j in range(3):
                send(ins, outs, sems, a, j, x, y, cc).start()

    def finish(ins, outs, sems):
        x, y, cc = _coords()
        for a in range(n):
            for j in range(3):
                landing(ins, outs, sems, a, j, x, y, cc).wait_recv()
        for a in range(n):
            for j in range(3):
                send(ins, outs, sems, a, j, x, y, cc).wait_send()
            own(ins, outs, sems, a, x, y).wait()

    dma = pltpu.SemaphoreType.DMA
    return _Comm(parts, [_sds(p.shape, p.dtype) for p in parts], [dma((n, 3)), dma((n, 3)), dma((n,))], start, finish)


def _sum_chips(q, cidx, *, name):
    nch, h, c = q.shape
    tr = _pick_rows(h)
    nbk = h // tr
    grid_spec = pltpu.PrefetchScalarGridSpec(
        num_scalar_prefetch=1, grid=(nbk,),
        in_specs=[pl.BlockSpec((nch, tr, c), lambda i, cref: (0, i, 0))],
        out_specs=pl.BlockSpec((tr, c), lambda i, cref: (cref[0] * nbk + i, 0)))

    def body(cref, q_ref, o_ref):
        acc = q_ref[0].astype(F32) + q_ref[1].astype(F32)
        acc = acc + q_ref[2].astype(F32)
        o_ref[...] = acc + q_ref[3].astype(F32)

    return _pcall(body, name=name, out_shape=_sds((2 * h, c), F32), grid_spec=grid_spec)(cidx, q)


def _join_comm(fulls):
    n = len(fulls)

    def half_copy(outs, sems, a, which):
        x, y, cc = _coords()
        h = outs[a].shape[0] // 2
        rows = outs[a].at[pl.ds((cc if which == 0 else 1 - cc) * h, h)]
        return pltpu.make_async_remote_copy(src_ref=rows, dst_ref=rows, send_sem=sems[0].at[a], recv_sem=sems[1].at[a],
                                            device_id=(x, y, 1 - cc), device_id_type=MESH)

    def start(ins, outs, sems):
        for a in range(n):
            half_copy(outs, sems, a, 0).start()

    def finish(ins, outs, sems):
        for a in range(n):
            half_copy(outs, sems, a, 1).wait_recv()
        for a in range(n):
            half_copy(outs, sems, a, 0).wait_send()

    dma = pltpu.SemaphoreType.DMA
    return _Comm(fulls, [_sds(p.shape, p.dtype) for p in fulls], [dma((n,)), dma((n,))], start, finish,
                 aliases={a: a for a in range(n)})


def _ada_prologue(c, w, b, *, name, comm=None):
    nb, dm = c.shape
    n = w.shape[1]

    def body(c_ref, w_ref, b_ref, call_ref, land_ref, cpad, mod_s, g_send, g_recv, m_send, m_recv):
        x, y, cc = _coords()
        me = 4 * x + 2 * y + cc
        chip = 2 * x + y
        cpad[...] = jnp.zeros_like(cpad)
        cpad[0:nb, :] = c_ref[...]
        copies = []
        for kk in range(1, N_DEV):
            peer = (x ^ (kk >> 2), y ^ ((kk >> 1) & 1), cc ^ (kk & 1))
            cp = pltpu.make_async_remote_copy(src_ref=cpad, dst_ref=call_ref.at[me], send_sem=g_send.at[kk - 1],
                                              recv_sem=g_recv.at[kk - 1], device_id=peer, device_id_type=MESH)
            cp.start()
            copies.append(cp)
        call_ref[me] = cpad[...]
        for kk in range(1, N_DEV):
            pltpu.make_async_remote_copy(src_ref=cpad, dst_ref=call_ref.at[me ^ kk], send_sem=g_send.at[kk - 1],
                                         recv_sem=g_recv.at[kk - 1], device_id=(x, y, cc), device_id_type=MESH).wait_recv()
        cv = call_ref[...].reshape(N_DEV * SUBLANE, dm)
        act = (cv * _sigmoid(cv)).astype(MX)
        mod = jnp.dot(act, w_ref[...].astype(MX), preferred_element_type=F32) + b_ref[...]
        mod_s[...] = mod.reshape(N_DEV, SUBLANE, n)
        for j, (dx, dy) in enumerate(CHIP_RELS):
            cp = pltpu.make_async_remote_copy(src_ref=mod_s.at[4 * (x ^ dx) + 2 * (y ^ dy) + cc], dst_ref=land_ref.at[chip],
                                              send_sem=m_send.at[j], recv_sem=m_recv.at[j],
                                              device_id=(x ^ dx, y ^ dy, cc), device_id_type=MESH)
            cp.start()
            copies.append(cp)
        land_ref[chip] = mod_s[me]
        for j, (dx, dy) in enumerate(CHIP_RELS):
            pltpu.make_async_remote_copy(src_ref=mod_s.at[me], dst_ref=land_ref.at[2 * (x ^ dx) + (y ^ dy)], send_sem=m_send.at[j],
                                         recv_sem=m_recv.at[j], device_id=(x, y, cc), device_id_type=MESH).wait_recv()
        for cp in copies:
            cp.wait_send()

    dma = pltpu.SemaphoreType.DMA
    return _pcall(body, name=name, in_specs=[VMEM_SPEC] * 3, out_specs=[VMEM_SPEC] * 2,
                  out_shape=[_sds((N_DEV, SUBLANE, dm), F32), _sds((N_CHIP, SUBLANE, n), F32)],
                  scratch=[pltpu.VMEM((SUBLANE, dm), F32), pltpu.VMEM((N_DEV, SUBLANE, n), F32),
                           dma((N_DEV - 1,)), dma((N_DEV - 1,)), dma((3,)), dma((3,))], comm=comm)(c, w, b)


def _ada_bwd(c_all, dmod_cols, *, name):
    m, kdim = c_all.shape
    n = dmod_cols.shape[1]
    tn = _pick(n, 1152)

    def body(c_ref, d_ref, o_ref):
        cv = c_ref[...]
        act = (cv * _sigmoid(cv)).astype(MX)
        o_ref[...] = lax.dot_general(act, d_ref[...].astype(MX), (((0,), (0,)), ((), ())), preferred_element_type=F32)

    return _pcall(body, name=name, out_shape=_sds((kdim, n), F32), grid=(n // tn,),
                  in_specs=[pl.BlockSpec((m, kdim), lambda j: (0, 0)), pl.BlockSpec((m, tn), lambda j: (0, j))],
                  out_specs=pl.BlockSpec((kdim, tn), lambda j: (0, j)))(c_all, dmod_cols)


SLAB_W = 1024
RED_ROWS = 8


LOSS_LANE = SLAB_W - LANE


def _pack_stats(st1, st2, st3, st_fin, st_q, st_kv, st_prep, d_pool_scale, loss8, *, name):
    nb = st1.shape[0]
    dm_rows = -(-9 * nb // SUBLANE) * SUBLANE

    def body(s1, s2, s3, sf, sq, skv, sp, sps, loss_ref, o_ref):
        o_ref[...] = jnp.zeros_like(o_ref)
        for s in range(nb):
            rows = [s1[s, 0:1, :], s1[s, 1:2, :], s2[s, 3:4, :], s2[s, 0:1, :], s2[s, 1:2, :], s3[s, 3:4, :], s3[s, 0:1, :],
                    s3[s, 1:2, :], sf[s, 0:1, :]]
            for k, row in enumerate(rows):
                o_ref[9 * s + k:9 * s + k + 1, :] = row

        def over_seq(ref, r):
            acc = ref[0, r:r + 1, :]
            for s in range(1, nb):
                acc = acc + ref[s, r:r + 1, :]
            return acc

        o_ref[dm_rows + 0:dm_rows + 1, :] = over_seq(s1, 2)
        o_ref[dm_rows + 1:dm_rows + 2, :] = over_seq(s2, 2)
        o_ref[dm_rows + 2:dm_rows + 3, :] = over_seq(s3, 2)
        o_ref[dm_rows + 3:dm_rows + 4, 0:POOL_W] = over_seq(sps, 0)
        o_ref[dm_rows + 4:dm_rows + 5, 0:QL] = over_seq(sq, 0)
        o_ref[dm_rows + 5:dm_rows + 6, 0:KVL] = over_seq(skv, 0)
        o_ref[dm_rows + 6:dm_rows + 7, 0:LANE] = over_seq(sp, 0)
        o_ref[dm_rows + 7:dm_rows + 8, 0:LANE] = over_seq(sp, 1)
        o_ref[dm_rows + 7:dm_rows + 8, LOSS_LANE:] = loss_ref[0:1, :]

    return _pcall(body, name=name, out_shape=_sds((dm_rows + RED_ROWS, SLAB_W), F32), in_specs=[VMEM_SPEC] * 9,
                  out_specs=VMEM_SPEC)(st1, st2, st3, st_fin, st_q, st_kv, st_prep, d_pool_scale, loss8)


def _small_allreduce(slab, pool, *, name, comm=None):
    rows, w = slab.shape
    dm = rows - RED_ROWS
    prow, pw = pool.shape
    crow = RED_ROWS + 2 * dm

    def body(slab_ref, pool_ref, red_ref, dmod_ref, ptot_ref, sib_slab, sib_pool, chip_slab, chip_pool, land_slab, land_pool,
             a_send, a_recv, b_send, b_recv):
        x, y, cc = _coords()
        chip = 2 * x + y
        sib = (x, y, 1 - cc)
        to_sib = [pltpu.make_async_remote_copy(src_ref=slab_ref, dst_ref=sib_slab, send_sem=a_send.at[0], recv_sem=a_recv.at[0],
                                               device_id=sib, device_id_type=MESH),
                  pltpu.make_async_remote_copy(src_ref=pool_ref, dst_ref=sib_pool, send_sem=a_send.at[1], recv_sem=a_recv.at[1],
                                               device_id=sib, device_id_type=MESH)]
        for cp in to_sib:
            cp.start()
        for cp in to_sib:
            cp.wait()
        mine_dm, theirs_dm = slab_ref[0:dm, :], sib_slab[0:dm, :]
        chip_slab[0:RED_ROWS, :] = slab_ref[dm:, :] + sib_slab[dm:, :]
        chip_slab[RED_ROWS:RED_ROWS + dm, :] = jnp.where(cc == 0, mine_dm, theirs_dm)
        chip_slab[RED_ROWS + dm:, :] = jnp.where(cc == 0, theirs_dm, mine_dm)
        chip_pool[...] = pool_ref[...] + sib_pool[...]

        sends = []
        for j, (dx, dy) in enumerate(CHIP_RELS):
            peer = (x ^ dx, y ^ dy, cc)
            sends.append(pltpu.make_async_remote_copy(src_ref=chip_slab, dst_ref=land_slab.at[chip], send_sem=b_send.at[j, 0],
                                                      recv_sem=b_recv.at[j, 0], device_id=peer, device_id_type=MESH))
            sends.append(pltpu.make_async_remote_copy(src_ref=chip_pool, dst_ref=land_pool.at[chip], send_sem=b_send.at[j, 1],
                                                      recv_sem=b_recv.at[j, 1], device_id=peer, device_id_type=MESH))
        for cp in sends:
            cp.start()
        land_slab[chip] = chip_slab[...]
        land_pool[chip] = chip_pool[...]
        for j, (dx, dy) in enumerate(CHIP_RELS):
            peer_chip = 2 * (x ^ dx) + (y ^ dy)
            pltpu.make_async_remote_copy(src_ref=chip_slab, dst_ref=land_slab.at[peer_chip], send_sem=b_send.at[j, 0],
                                         recv_sem=b_recv.at[j, 0], device_id=(x, y, cc), device_id_type=MESH).wait_recv()
            pltpu.make_async_remote_copy(src_ref=chip_pool, dst_ref=land_pool.at[peer_chip], send_sem=b_send.at[j, 1],
                                         recv_sem=b_recv.at[j, 1], device_id=(x, y, cc), device_id_type=MESH).wait_recv()
        red = land_slab[0, 0:RED_ROWS, :]
        ptot = land_pool[0]
        for ch in range(1, N_CHIP):
            red = red + land_slab[ch, 0:RED_ROWS, :]
            ptot = ptot + land_pool[ch]
        red_ref[...] = red
        ptot_ref[...] = ptot
        for ch in range(N_CHIP):
            dmod_ref[2 * dm * ch:2 * dm * (ch + 1), :] = land_slab[ch, RED_ROWS:, :]
        for cp in sends:
            cp.wait_send()

    dma = pltpu.SemaphoreType.DMA
    return _pcall(body, name=name, in_specs=[VMEM_SPEC, VMEM_SPEC], out_specs=[VMEM_SPEC] * 3,
                  out_shape=[_sds((RED_ROWS, w), F32), _sds((N_DEV * dm, w), F32), _sds((prow, pw), F32)],
                  scratch=[pltpu.VMEM((rows, w), F32), pltpu.VMEM((prow, pw), F32), pltpu.VMEM((crow, w), F32),
                           pltpu.VMEM((prow, pw), F32), pltpu.VMEM((N_CHIP, crow, w), F32), pltpu.VMEM((N_CHIP, prow, pw), F32),
                           dma((2,)), dma((2,)), dma((3, 2)), dma((3, 2))], comm=comm)(slab, pool)


def _adamw_math(w, g, m, v):
    mn = ADAM_B1 * m + (1.0 - ADAM_B1) * g
    vn = ADAM_B2 * v + (1.0 - ADAM_B2) * (g * g)
    bc1 = 1.0 / (1.0 - ADAM_B1 ** ADAM_STEP)
    bc2 = 1.0 / (1.0 - ADAM_B2 ** ADAM_STEP)
    return -ADAM_LR * ((mn * bc1) / (jnp.sqrt(vn * bc2) + ADAM_EPS) + ADAM_WD * w), mn, vn


def _small_update(red, dmod_all, pool_total, nb, params, *, name):
    names = list(SMALL)
    dm = dmod_all.shape[0] // N_DEV

    def grad_of(nm, red_ref, dmod_ref, ptot_ref):
        if nm == "b_ada":
            acc = None
            for d in range(N_DEV):
                for s in range(nb):
                    blk = dmod_ref[d * dm + 9 * s:d * dm + 9 * s + 9, :]
                    acc = blk if acc is None else acc + blk
            return jnp.concatenate([acc[k:k + 1, :] for k in range(9)], axis=1)
        if nm == "pool_grp":
            return ptot_ref[...]
        row, lo, n = {"norm_ffn1": (0, 0, D), "norm_mix": (1, 0, D), "norm_ffn2": (2, 0, D), "pool_scale": (3, 0, POOL_W),
                      "q_a_norm": (4, 0, QL), "kv_a_norm": (5, 0, KVL), "q_norm_nope": (6, 0, NOPE),
                      "q_norm_rope": (6, NOPE, ROPE), "k_norm_nope": (7, 0, NOPE), "k_norm_rope": (7, KR_LANE, ROPE)}[nm]
        return red_ref[row:row + 1, lo:lo + n]

    def body(*refs):
        red_ref, dmod_ref, ptot_ref = refs[:3]
        ins = refs[3:3 + 3 * len(names)]
        outs = refs[3 + 3 * len(names):]
        outs[4 * len(names)][...] = red_ref[RED_ROWS - 1:RED_ROWS, LOSS_LANE:]
        for i, nm in enumerate(names):
            g = grad_of(nm, red_ref, dmod_ref, ptot_ref)
            d, mn, vn = _adamw_math(ins[3 * i][...], g, ins[3 * i + 1][...], ins[3 * i + 2][...])
            outs[4 * i][...] = g
            outs[4 * i + 1][...] = d
            outs[4 * i + 2][...] = mn
            outs[4 * i + 3][...] = vn

    flat_in = [a for nm in names for a in params[nm]]
    out_shape = [_sds(params[nm][0].shape, F32) for nm in names for _ in range(4)] + [_sds((1, LANE), F32)]
    res = _pcall(body, name=name, in_specs=[VMEM_SPEC] * (3 + len(flat_in)), out_specs=[VMEM_SPEC] * len(out_shape),
                 out_shape=out_shape)(red, dmod_all, pool_total, *flat_in)
    return {nm: tuple(res[4 * i:4 * i + 4]) for i, nm in enumerate(names)}, res[-1]


def _adamw(w, g, m, v, *, name, comm=None):
    r, c = w.shape
    tr = _pick_rows(r, 256)
    tc = c if tr < r else _pick(c, 256)
    spec = pl.BlockSpec((tr, tc), lambda i, j: (i, j))
    bc1 = 1.0 / (1.0 - ADAM_B1 ** ADAM_STEP)
    bc2 = 1.0 / (1.0 - ADAM_B2 ** ADAM_STEP)

    def body(w_ref, g_ref, m_ref, v_ref, d_ref, mo_ref, vo_ref):
        gv = g_ref[...]
        mn = ADAM_B1 * m_ref[...] + (1.0 - ADAM_B1) * gv
        vn = ADAM_B2 * v_ref[...] + (1.0 - ADAM_B2) * (gv * gv)
        mo_ref[...] = mn
        vo_ref[...] = vn
        d_ref[...] = -ADAM_LR * ((mn * bc1) / (jnp.sqrt(vn * bc2) + ADAM_EPS) + ADAM_WD * w_ref[...])

    out = _sds((r, c), F32)
    return _pcall(body, name=name, out_shape=[out, out, out], grid=(r // tr, c // tc), in_specs=[spec] * 4, out_specs=[spec] * 3,
                  comm=comm)(w, g, m, v)


BIG = ("w_ffn1_in", "w_ffn1_out", "w_in", "w_pool_proj", "w_q_up", "w_kv_up", "w_mla_proj", "w_out", "w_ffn2_in", "w_ffn2_out")
ROW_SHARDED = ("w_ffn1_out", "w_out", "w_ffn2_out")
KERNEL_NAME = {"w_ffn1_in": "ffn1_in", "w_ffn1_out": "ffn1_out", "w_in": "w_in", "w_pool_proj": "pool_proj", "w_q_up": "q_up",
               "w_kv_up": "kv_up", "w_mla_proj": "mla_proj", "w_out": "w_out", "w_ffn2_in": "ffn2_in", "w_ffn2_out": "ffn2_out"}
WEIGHTS = ("w_ada", "b_ada", "norm_ffn1", "w_ffn1_in", "w_ffn1_out", "norm_mix", "w_in", "pool_grp", "pool_scale", "w_pool_proj",
           "q_a_norm", "w_q_up", "kv_a_norm", "w_kv_up", "q_norm_nope", "q_norm_rope", "k_norm_nope", "k_norm_rope", "w_mla_proj",
           "w_out", "norm_ffn2", "w_ffn2_in", "w_ffn2_out")
SMALL = ("b_ada", "norm_ffn1", "norm_mix", "pool_grp", "pool_scale", "q_a_norm", "kv_a_norm", "q_norm_nope", "q_norm_rope",
         "k_norm_nope", "k_norm_rope", "norm_ffn2")


def _assemble(name, stacked):
    if name in ROW_SHARDED:
        return stacked.reshape(stacked.shape[0] * stacked.shape[1], stacked.shape[2])
    return jnp.transpose(stacked, (1, 0, 2)).reshape(stacked.shape[1], stacked.shape[0] * stacked.shape[2])


def _split(name, full):
    if name in ROW_SHARDED:
        return full.reshape(N_CHIP, full.shape[0] // N_CHIP, full.shape[1])
    return jnp.transpose(full.reshape(full.shape[0], N_CHIP, full.shape[1] // N_CHIP), (1, 0, 2))


GU = ("w_ffn1_in", "w_ffn2_in")
NARROW = ("w_in", "w_q_up")


MIX_SMALL = ("w_out", "w_pool_proj", "w_mla_proj", "w_q_up", "w_kv_up")
RIDES = {
    "ada_prologue": (("gather", ("w_ffn1_in",)),),
    "ffn1_up_a": (("gather", ("w_ffn1_out",)),),
    "ffn1_up_b": (("gather", ("w_in",)),),
    "mix_in": (("gather", MIX_SMALL),),
    "attn_fwd": (("gather", ("w_ffn2_in", "w_ffn2_out")),),
    "d_ffn2_h": (("swap", ("w_ffn2_out", "w_ffn2_in")),),
    "attn_bwd": (("exchange", ("w_ffn2_out", "w_ffn2_in")),),
    "d_mix_in": (("swap", MIX_SMALL),),
    "d_mix_h": (("exchange", MIX_SMALL), ("swap", ("w_in",))),
    "ffn1_dact": (("exchange", ("w_in",)), ("swap", ("w_ffn1_out",))),
    "d_ffn1_in_a": (("exchange", ("w_ffn1_out",)),),
    "d_ffn1_in_b": (("swap", ("w_ffn1_in@0",)),),
    "d_ffn1_h": (("exchange", ("w_ffn1_in@0",)), ("swap", ("w_ffn1_in@1",))),
    "bwd_norm1": (("exchange", ("w_ffn1_in@1",)),),
    "small_allreduce": (("join", tuple(n for n in BIG if n != "w_ffn1_in") + ("w_ffn1_in@0", "w_ffn1_in@1")),),
}


def _kname(n):
    base, _, part = n.partition("@")
    return KERNEL_NAME[base] + ("@" + part if part else "")


def _both(comms):
    if len(comms) == 1:
        return comms[0]
    ins, outs, sems, aliases, spans = [], [], [], {}, []
    for c in comms:
        spans.append((len(ins), len(c.ins), len(outs), len(c.out_shapes), len(sems), len(c.sems)))
        aliases.update({len(ins) + i: len(outs) + o for i, o in c.aliases.items()})
        ins, outs, sems = ins + c.ins, outs + c.out_shapes, sems + c.sems

    def each(which):
        def run(i_, o_, s_):
            for c, (ia, ni, oa, no, sa, ns) in zip(comms, spans):
                getattr(c, which)(i_[ia:ia + ni], o_[oa:oa + no], s_[sa:sa + ns])
        return run

    return _Comm(ins, outs, sems, each("start"), each("finish"), aliases)


class _ExchangePlan:
    def __init__(self, shards, cidx):
        self.shards, self.cidx = shards, cidx
        self.W, self.G, self.parts, self.pre, self.reduced, self.joined = {}, {}, {}, {}, {}, {}

    def grad(self, key, g):
        self.G[key] = g

    def rider(self, name):
        comms = []
        for kind, names in RIDES.get(name, ()):
            if kind == "gather":
                comms.append(_gather_comm([self.shards[n] for n in names]))
            elif kind == "swap":
                for n in names:
                    self.parts[n] = self._stacked(n)
                comms.append(_swap_comm([self.parts[n] for n in names]))
            elif kind == "exchange":
                comms.append(_exchange_comm([self.pre[n] for n in names]))
            else:
                comms.append(_join_comm([self.reduced[n] for n in names]))
        return _both(comms) if comms else None

    def landed(self, name, outs):
        at = 0
        for kind, names in RIDES[name]:
            for n, o in zip(names, outs[at:at + len(names)]):
                if kind == "gather":
                    self.W[KERNEL_NAME[n]] = self._to_kernel(n, o)
                elif kind == "swap":
                    self.pre[n] = _add_half(self.parts[n], o, self.cidx, name="rs_add_" + _kname(n))
                elif kind == "exchange":
                    self.reduced[n] = _sum_chips(o, self.cidx, name="rs_sum_" + _kname(n))
                else:
                    self.joined[n] = o
            at += len(names)

    @staticmethod
    def _to_kernel(n, stacked):
        if n in GU:
            return stacked
        full = _assemble(n, stacked)
        return {"w_in": _w_in_to_kernel, "w_q_up": _q_up_to_kernel, "w_kv_up": _kv_up_to_kernel}.get(n, lambda w: w)(full)

    def _stacked(self, n):
        g = self.G[_kname(n)]
        if n.partition("@")[0] in GU:
            return g
        full = {"w_in": _w_in_from_kernel, "w_q_up": _q_up_from_kernel, "w_kv_up": _kv_up_from_kernel}.get(n, lambda w: w)(g)
        return _split(n, full)


def kernel(x, c, positions, w_ada, b_ada, norm_ffn1, w_ffn1_in, w_ffn1_out, norm_mix, w_in, pool_grp, pool_scale, w_pool_proj, q_a_norm, w_q_up, kv_a_norm, w_kv_up, q_norm_nope, q_norm_rope, k_norm_nope, k_norm_rope, w_mla_proj, w_out, norm_ffn2, w_ffn2_in, w_ffn2_out, loss_target, m_w_ada, m_b_ada, m_norm_ffn1, m_w_ffn1_in, m_w_ffn1_out, m_norm_mix, m_w_in, m_pool_grp, m_pool_scale, m_w_pool_proj, m_q_a_norm, m_w_q_up, m_kv_a_norm, m_w_kv_up, m_q_norm_nope, m_q_norm_rope, m_k_norm_nope, m_k_norm_rope, m_w_mla_proj, m_w_out, m_norm_ffn2, m_w_ffn2_in, m_w_ffn2_out, v_w_ada, v_b_ada, v_norm_ffn1, v_w_ffn1_in, v_w_ffn1_out, v_norm_mix, v_w_in, v_pool_grp, v_pool_scale, v_w_pool_proj, v_q_a_norm, v_w_q_up, v_kv_a_norm, v_w_kv_up, v_q_norm_nope, v_q_norm_rope, v_k_norm_nope, v_k_norm_rope, v_w_mla_proj, v_w_out, v_norm_ffn2, v_w_ffn2_in, v_w_ffn2_out):
    args = dict(locals())
    wts = {n: args[n][0] for n in WEIGHTS}
    mom = {n: args["m_" + n][0] for n in WEIGHTS}
    var = {n: args["v_" + n][0] for n in WEIGHTS}
    nb, seq, dm = x.shape
    tokens = nb * seq
    xi, yi, ci = _coords()
    chip = 2 * xi + yi

    cidx = ci.astype(jnp.int32).reshape(1)
    plan = _ExchangePlan({n: wts[n].astype(MX) for n in BIG}, cidx)
    plan.W["pool_grp"] = wts["pool_grp"].astype(MX)

    ncol = w_ada.shape[2]
    b_cols = lax.dynamic_slice_in_dim(wts["b_ada"].reshape(1, -1), chip * ncol, ncol, axis=1)
    c_slots, mod_slots = _run(plan, _ada_prologue, c, wts["w_ada"], b_cols, name="ada_prologue")
    c_all = c_slots[:, :nb].reshape(N_DEV * nb, dm)
    mod3 = jnp.transpose(mod_slots[:, :nb], (1, 0, 2)).reshape(nb, 9, dm)
    P = {"norm_ffn1": wts["norm_ffn1"].reshape(1, dm), "norm_mix": wts["norm_mix"].reshape(1, dm),
         "norm_ffn2": wts["norm_ffn2"].reshape(1, dm), "pool_scale": wts["pool_scale"].reshape(1, POOL_W),
         "q_a_norm": wts["q_a_norm"].reshape(1, QL), "kv_a_norm": wts["kv_a_norm"].reshape(1, KVL),
         "q_gain": _gain_slab(wts["q_norm_nope"].reshape(1, NOPE), wts["q_norm_rope"].reshape(1, ROPE)),
         "k_gain": _gain_slab(wts["k_norm_nope"].reshape(1, NOPE), wts["k_norm_rope"].reshape(1, ROPE))}
    cos, sin = _rope_tables(positions.reshape(tokens))

    loss8, grad_x, stats, d_pool_grp = _layer_fwd_bwd(x.reshape(tokens, dm), loss_target.reshape(tokens, dm), mod3, cos, sin, plan, P)

    slab = _pack_stats(*stats, loss8, name="pack_stats")
    red, dmod_rows, pool_total = _run(plan, _small_allreduce, slab, d_pool_grp.reshape(POOL_G * LANE, LANE), name="small_allreduce")
    grads = {n: plan.joined[n] for n in BIG if n != "w_ffn1_in"}
    grads["w_ffn1_in"] = jnp.concatenate([plan.joined["w_ffn1_in@0"], plan.joined["w_ffn1_in@1"]], axis=0)
    dm_rows = dmod_rows.shape[0] // N_DEV
    dmod_all = dmod_rows.reshape(N_DEV, dm_rows, SLAB_W)[:, :9 * nb].reshape(N_DEV * nb, 9 * dm)
    dmod_cols = lax.dynamic_slice_in_dim(dmod_all, chip * ncol, ncol, axis=1)
    grads["w_ada"] = _ada_bwd(c_all, dmod_cols, name="ada_bwd")

    delta, new_m, new_v = {}, {}, {}
    as2d = lambda a: a.reshape(POOL_G * LANE, LANE) if a.ndim == 4 else a.reshape(1, -1)
    upd, loss_row = _small_update(red, dmod_rows, pool_total, nb,
                                  {n: tuple(as2d(args[p + n]) for p in ("", "m_", "v_")) for n in SMALL}, name="small_update")
    loss = loss_row[0, 0]
    for n in SMALL:
        grads[n], delta[n], new_m[n], new_v[n] = upd[n]
    for n in ("w_ada",) + BIG:
        if n in NARROW:
            res = _adamw(wts[n].T, grads[n].T, mom[n].T, var[n].T, name="adamw_" + n)
            delta[n], new_m[n], new_v[n] = (r.T for r in res)
        else:
            delta[n], new_m[n], new_v[n] = _adamw(wts[n], grads[n], mom[n], var[n], name="adamw_" + n)

    def lead(a, n):
        return a.reshape((1,) + wts[n].shape)

    return (loss, grad_x.reshape(nb, seq, dm), *[lead(grads[n], n) for n in WEIGHTS], *[lead(delta[n], n) for n in WEIGHTS],
            *[lead(new_m[n], n) for n in WEIGHTS], *[lead(new_v[n], n) for n in WEIGHTS])
```

```python
import functools
import math

import jax
import jax.numpy as jnp
from jax import lax
from jax.experimental import pallas as pl
from jax.experimental.pallas import tpu as pltpu

F32 = jnp.float32
MX = jnp.bfloat16

D = 1024
DFF = 2816
NH = 8
POOL_W = 512
POOL_G = 4
QL = 384
KVL = 256
ROPE = 32
NOPE = 64
LANE = 128
SUBLANE = 8
EPS = 1e-6
ATTN_SCALE = 1.0 / math.sqrt(96.0)
NEG = -1e30

Z_UP, Z_QL, Z_KV, Z_KR, Z_GP, Z_GM, Z_W = 0, 512, 896, 1152, 1280, 2304, 3328
KR_LANE = 64
LAT_W = 1280

ADAM_LR, ADAM_B1, ADAM_B2, ADAM_EPS, ADAM_WD, ADAM_STEP = 0.001, 0.9, 0.999, 1e-08, 0.01, 10

VMEM_LIMIT = 48 * 1024 * 1024
MESH = pl.DeviceIdType.MESH
N_DEV = 8
N_CHIP = 4


class _Comm:
    def __init__(self, ins, out_shapes, sems, start, finish, aliases=None):
        self.ins, self.out_shapes, self.sems = list(ins), list(out_shapes), list(sems)
        self.start, self.finish = start, finish
        self.aliases = aliases or {}


def _pcall(body, *, name, out_shape, grid=(), in_specs=None, out_specs=None, scratch=(), grid_spec=None, aliases=None,
           comm=None):
    params = pltpu.CompilerParams(vmem_limit_bytes=VMEM_LIMIT)
    kw = dict(name=name, compiler_params=params)
    if comm is None:
        if aliases:
            kw["input_output_aliases"] = aliases
        if grid_spec is not None:
            return pl.pallas_call(body, grid_spec=grid_spec, out_shape=out_shape, **kw)
        return pl.pallas_call(body, grid=grid, in_specs=in_specs, out_specs=out_specs, scratch_shapes=scratch,
                              out_shape=out_shape, **kw)
    single = not isinstance(out_shape, (list, tuple))
    outs = [out_shape] if single else list(out_shape)
    ospecs = [out_specs] if single else list(out_specs)
    n_in, n_out, n_ci, n_co, n_scr = len(in_specs), len(outs), len(comm.ins), len(comm.out_shapes), len(scratch)
    io = dict(aliases or {})
    io.update({n_in + i: n_out + o for i, o in comm.aliases.items()})

    def riding(*refs):
        ins, cins = refs[:n_in], refs[n_in:n_in + n_ci]
        at = n_in + n_ci
        os_, couts = refs[at:at + n_out], refs[at + n_out:at + n_out + n_co]
        at += n_out + n_co
        scr, csems = refs[at:at + n_scr], refs[at + n_scr:]
        if grid:
            first = functools.reduce(jnp.logical_and, [pl.program_id(d) == 0 for d in range(len(grid))])
            last = functools.reduce(jnp.logical_and, [pl.program_id(d) == grid[d] - 1 for d in range(len(grid))])
            pl.when(first)(lambda: comm.start(cins, couts, csems))
            body(*ins, *os_, *scr)
            pl.when(last)(lambda: comm.finish(cins, couts, csems))
        else:
            comm.start(cins, couts, csems)
            body(*ins, *os_, *scr)
            comm.finish(cins, couts, csems)

    call = pl.pallas_call(riding, grid=grid, in_specs=list(in_specs) + [HBM_SPEC] * n_ci, out_specs=ospecs + [HBM_SPEC] * n_co,
                          out_shape=outs + comm.out_shapes, scratch_shapes=list(scratch) + comm.sems,
                          input_output_aliases=io, **kw)

    def run(*args):
        res = call(*args, *comm.ins)
        main = list(res[:n_out])
        return (main[0] if single else main), list(res[n_out:])

    return run


def _pick(dim, target):
    best = None
    for t in range(LANE, min(dim, target) + 1, LANE):
        if dim % t == 0:
            best = t
    return dim if best is None else best


def _sds(shape, dtype):
    return jax.ShapeDtypeStruct(shape, dtype)


def _dw(a, g, *, name, tm=512, tn=1024, comm=None):
    return _mm(a, g, mode="tn", out_dtype=F32, name=name, tm=tm, tn=tn, tk=a.shape[0], n_outer=True, comm=comm)


def _mm(a, b, *, mode, out_dtype, name, tm=1024, tn=1024, tk=4096, n_outer=False, comm=None):
    if mode == "nn":
        (M, K), (K2, N) = a.shape, b.shape
    elif mode == "nt":
        (M, K), (N, K2) = a.shape, b.shape
    else:
        (K, M), (K2, N) = a.shape, b.shape
    assert K == K2, (name, a.shape, b.shape)
    tm, tn, tk = _pick(M, tm), _pick(N, tn), _pick(K, tk)
    nk = K // tk
    if n_outer:
        ij = lambda g0, g1: (g1, g0)
        grid = (N // tn, M // tm, nk)
    else:
        ij = lambda g0, g1: (g0, g1)
        grid = (M // tm, N // tn, nk)
    if mode == "tn":
        a_spec = pl.BlockSpec((tk, tm), lambda g0, g1, k: (k, ij(g0, g1)[0]))
    else:
        a_spec = pl.BlockSpec((tm, tk), lambda g0, g1, k: (ij(g0, g1)[0], k))
    if mode == "nt":
        b_spec = pl.BlockSpec((tn, tk), lambda g0, g1, k: (ij(g0, g1)[1], k))
    else:
        b_spec = pl.BlockSpec((tk, tn), lambda g0, g1, k: (k, ij(g0, g1)[1]))
    o_spec = pl.BlockSpec((tm, tn), lambda g0, g1, k: ij(g0, g1))
    dn = {"nn": (((1,), (0,)), ((), ())), "nt": (((1,), (1,)), ((), ())), "tn": (((0,), (0,)), ((), ()))}[mode]

    def dot(a_ref, b_ref):
        return lax.dot_general(a_ref[...].astype(MX), b_ref[...].astype(MX), dn, preferred_element_type=F32)

    def body_one(a_ref, b_ref, o_ref):
        o_ref[...] = dot(a_ref, b_ref).astype(o_ref.dtype)

    def body_acc(a_ref, b_ref, o_ref, acc_ref):
        k = pl.program_id(2)
        part = dot(a_ref, b_ref)

        @pl.when(k == 0)
        def _():
            acc_ref[...] = part

        @pl.when(k > 0)
        def _():
            acc_ref[...] += part

        @pl.when(k == nk - 1)
        def _():
            o_ref[...] = acc_ref[...].astype(o_ref.dtype)

    return _pcall(body_one if nk == 1 else body_acc, name=name, out_shape=_sds((M, N), out_dtype), grid=grid,
                  in_specs=[a_spec, b_spec], out_specs=o_spec, scratch=[] if nk == 1 else [pltpu.VMEM((tm, tn), F32)],
                  comm=comm)(a, b)


def _gu_shard(q):
    return (q % 2) * 2 + q // 2


def _ffn_up(h, w_st, *, name, tm=512, comm=None):
    T, dm = h.shape
    hw = w_st.shape[2]
    tm = _pick(T, tm)

    def body(h_ref, wg_ref, wu_ref, gu_ref, a_ref):
        hv = h_ref[...]
        g = jnp.dot(hv, wg_ref[0], preferred_element_type=F32)
        u = jnp.dot(hv, wu_ref[0], preferred_element_type=F32)
        gu_ref[:, :hw] = g.astype(gu_ref.dtype)
        gu_ref[:, hw:] = u.astype(gu_ref.dtype)
        a_ref[...] = (g * _sigmoid(g) * u).astype(a_ref.dtype)

    return _pcall(body, name=name, grid=(T // tm, 2),
                  in_specs=[pl.BlockSpec((tm, dm), lambda i, j: (i, 0)), pl.BlockSpec((1, dm, hw), lambda i, j: (j, 0, 0)),
                            pl.BlockSpec((1, dm, hw), lambda i, j: (2 + j, 0, 0))],
                  out_specs=[pl.BlockSpec((tm, 2 * hw), lambda i, j: (i, j)), pl.BlockSpec((tm, hw), lambda i, j: (i, j))],
                  out_shape=[_sds((T, 4 * hw), MX), _sds((T, 2 * hw), MX)], comm=comm)(h, w_st, w_st)


def _ffn_up_first(x, mod3, gain, w_st, *, sub, name, tm=512, comm=None):
    T, dm = x.shape
    hw = w_st.shape[2]
    tm = _pick(T, tm)
    tps = (T // mod3.shape[0]) // tm

    def body(x_ref, mod_ref, n_ref, wg_ref, wu_ref, h_ref, gu_ref, a_ref):
        xv = x_ref[...]
        xn = xv * _rsq(xv) * n_ref[...]
        hv = (xn * (1.0 + mod_ref[0, 3 * sub + 1:3 * sub + 2, :]) + mod_ref[0, 3 * sub:3 * sub + 1, :]).astype(h_ref.dtype)
        h_ref[...] = hv
        g = jnp.dot(hv, wg_ref[0], preferred_element_type=F32)
        u = jnp.dot(hv, wu_ref[0], preferred_element_type=F32)
        gu_ref[:, :hw] = g.astype(gu_ref.dtype)
        gu_ref[:, hw:] = u.astype(gu_ref.dtype)
        a_ref[...] = (g * _sigmoid(g) * u).astype(a_ref.dtype)

    return _pcall(body, name=name, grid=(T // tm,),
                  in_specs=[_row_spec(tm, dm), pl.BlockSpec((1, 9, dm), lambda i: (i // tps, 0, 0)),
                            pl.BlockSpec((1, dm), lambda i: (0, 0)), pl.BlockSpec((1, dm, hw), lambda i: (0, 0, 0)),
                            pl.BlockSpec((1, dm, hw), lambda i: (2, 0, 0))],
                  out_specs=[_row_spec(tm, dm), pl.BlockSpec((tm, 2 * hw), lambda i: (i, 0)), pl.BlockSpec((tm, hw), lambda i: (i, 0))],
                  out_shape=[_sds((T, dm), MX), _sds((T, 4 * hw), MX), _sds((T, 2 * hw), MX)],
                  comm=comm)(x, mod3, gain, w_st, w_st)


def _ffn_up_second(h, w_st, gu, a, *, name, tm=512, comm=None):
    T, dm = h.shape
    hw = w_st.shape[2]
    tm = _pick(T, tm)

    def body(h_ref, wg_ref, wu_ref, gu_in, a_in, gu_ref, a_ref):
        hv = h_ref[...]
        g = jnp.dot(hv, wg_ref[0], preferred_element_type=F32)
        u = jnp.dot(hv, wu_ref[0], preferred_element_type=F32)
        gu_ref[:, :hw] = g.astype(gu_ref.dtype)
        gu_ref[:, hw:] = u.astype(gu_ref.dtype)
        a_ref[...] = (g * _sigmoid(g) * u).astype(a_ref.dtype)

    return _pcall(body, name=name, grid=(T // tm,),
                  in_specs=[_row_spec(tm, dm), pl.BlockSpec((1, dm, hw), lambda i: (1, 0, 0)),
                            pl.BlockSpec((1, dm, hw), lambda i: (3, 0, 0)), HBM_SPEC, HBM_SPEC],
                  out_specs=[pl.BlockSpec((tm, 2 * hw), lambda i: (i, 1)), pl.BlockSpec((tm, hw), lambda i: (i, 1))],
                  out_shape=[_sds(gu.shape, gu.dtype), _sds(a.shape, a.dtype)], aliases={3: 0, 4: 1},
                  comm=comm)(h, w_st, w_st, gu, a)


def _ffn_dact(df, gu, w_out, *, name, tm=512, comm=None):
    T, dm = df.shape
    hw = gu.shape[1] // 4
    tm = _pick(T, tm)

    def body(df_ref, gu_ref, wo_ref, dgu_ref):
        da = lax.dot_general(df_ref[...], wo_ref[...], (((1,), (1,)), ((), ())), preferred_element_type=F32)
        g = gu_ref[:, :hw].astype(F32)
        u = gu_ref[:, hw:].astype(F32)
        s = _sigmoid(g)
        dgu_ref[:, :hw] = (da * u * (s * (1.0 + g * (1.0 - s)))).astype(dgu_ref.dtype)
        dgu_ref[:, hw:] = (da * (g * s)).astype(dgu_ref.dtype)

    return _pcall(body, name=name, grid=(T // tm, 2),
                  in_specs=[pl.BlockSpec((tm, dm), lambda i, j: (i, 0)), pl.BlockSpec((tm, 2 * hw), lambda i, j: (i, j)),
                            pl.BlockSpec((hw, dm), lambda i, j: (j, 0))],
                  out_specs=pl.BlockSpec((tm, 2 * hw), lambda i, j: (i, j)), out_shape=_sds(gu.shape, MX),
                  comm=comm)(df, gu, w_out)


def _ffn_dh(dgu, w_st, *, name, tm=1024, comm=None):
    T = dgu.shape[0]
    _, dm, hw = w_st.shape
    tm = _pick(T, tm)

    def body(d_ref, w_ref, o_ref, acc_ref):
        q = pl.program_id(1)
        part = lax.dot_general(d_ref[...], w_ref[0], (((1,), (1,)), ((), ())), preferred_element_type=F32)

        @pl.when(q == 0)
        def _():
            acc_ref[...] = part

        @pl.when(jnp.logical_and(q > 0, q < 3))
        def _():
            acc_ref[...] += part

        @pl.when(q == 3)
        def _():
            o_ref[...] = acc_ref[...] + part

    return _pcall(body, name=name, grid=(T // tm, 4),
                  in_specs=[pl.BlockSpec((tm, hw), lambda i, q: (i, q)), pl.BlockSpec((1, dm, hw), lambda i, q: (_gu_shard(q), 0, 0))],
                  out_specs=pl.BlockSpec((tm, dm), lambda i, q: (i, 0)), out_shape=_sds((T, dm), F32),
                  scratch=[pltpu.VMEM((tm, dm), F32)], comm=comm)(dgu, w_st)


def _ffn_dw_in(h, dgu, *, name, rows=None, tm=512, comm=None):
    T, dm = h.shape
    hw = dgu.shape[1] // 4
    first, count = rows if rows is not None else (0, dm)
    tm = _pick(count, tm)
    skip = first // tm

    def body(h_ref, d_ref, o_ref):
        o_ref[0] = lax.dot_general(h_ref[...], d_ref[...], (((0,), (0,)), ((), ())), preferred_element_type=F32)

    return _pcall(body, name=name, grid=(4, count // tm),
                  in_specs=[pl.BlockSpec((T, tm), lambda q, i: (0, skip + i)), pl.BlockSpec((T, hw), lambda q, i: (0, q))],
                  out_specs=pl.BlockSpec((1, tm, hw), lambda q, i: (_gu_shard(q), i, 0)),
                  out_shape=_sds((4, count, hw), F32), comm=comm)(h, dgu)


def _rsq(x):
    return lax.rsqrt(jnp.mean(x * x, axis=-1, keepdims=True) + EPS)


def _sigmoid(x):
    return 1.0 / (1.0 + jnp.exp(-x))


def _row_spec(tm, w):
    return pl.BlockSpec((tm, w), lambda i: (i, 0))


def _mm_resid_norm(a, w, x_prev, mod3, gain, *, sub, coef, name, tm=512, comm=None):
    T, k = a.shape
    dm = w.shape[1]
    tm = _pick(T, tm)
    tps = (T // mod3.shape[0]) // tm

    def body(a_ref, w_ref, x_ref, mod_ref, n_ref, f_ref, xo_ref, h_ref):
        f = jnp.dot(a_ref[...], w_ref[...], preferred_element_type=F32)
        f_ref[...] = f
        x = x_ref[...] + coef * mod_ref[0, 3 * sub - 1:3 * sub, :] * f
        xo_ref[...] = x
        xn = x * _rsq(x) * n_ref[...]
        h_ref[...] = (xn * (1.0 + mod_ref[0, 3 * sub + 1:3 * sub + 2, :]) + mod_ref[0, 3 * sub:3 * sub + 1, :]).astype(h_ref.dtype)

    row = _row_spec(tm, dm)
    return _pcall(body, name=name, grid=(T // tm,),
                  in_specs=[_row_spec(tm, k), pl.BlockSpec((k, dm), lambda i: (0, 0)), row,
                            pl.BlockSpec((1, 9, dm), lambda i: (i // tps, 0, 0)), pl.BlockSpec((1, dm), lambda i: (0, 0))],
                  out_specs=[row, row, row], out_shape=[_sds((T, dm), F32), _sds((T, dm), F32), _sds((T, dm), MX)],
                  comm=comm)(a, w, x_prev, mod3, gain)


def _mm_loss(a, w, x2, tgt, mod3, *, name, tm=512):
    T, k = a.shape
    dm = w.shape[1]
    tm = _pick(T, tm)
    tps = (T // mod3.shape[0]) // tm
    mod_spec = pl.BlockSpec((1, 9, dm), lambda i: (i // tps, 0, 0))
    row = _row_spec(tm, dm)
    stat_spec = pl.BlockSpec((1, SUBLANE, dm), lambda i: (i // tps, 0, 0))
    loss_spec = pl.BlockSpec((SUBLANE, LANE), lambda i: (0, 0))

    def body(a_ref, w_ref, x_ref, t_ref, mod_ref, dy_ref, df_ref, st_ref, loss_ref):
        i = pl.program_id(0)
        g = mod_ref[0, 8:9, :]
        f = jnp.dot(a_ref[...], w_ref[...], preferred_element_type=F32)
        err = x_ref[...] + 0.5 * g * f - t_ref[...]
        dy = err * (1.0 / dm)
        dy_ref[...] = dy
        df_ref[...] = (0.5 * g * dy).astype(df_ref.dtype)
        dgate = jnp.sum(0.5 * dy * f, axis=0, keepdims=True)
        part = 0.5 * jnp.sum(jnp.sum(err * err, axis=0, keepdims=True), axis=1, keepdims=True) * (1.0 / dm)

        @pl.when(i % tps == 0)
        def _():
            st_ref[...] = jnp.zeros_like(st_ref)

        @pl.when(i == 0)
        def _():
            loss_ref[...] = jnp.zeros_like(loss_ref)

        st_ref[0, 0:1, :] += dgate
        loss_ref[...] += jnp.broadcast_to(part, loss_ref.shape)

    return _pcall(body, name=name, grid=(T // tm,),
                  in_specs=[_row_spec(tm, k), pl.BlockSpec((k, dm), lambda i: (0, 0)), row, row, mod_spec],
                  out_specs=[row, row, stat_spec, loss_spec],
                  out_shape=[_sds((T, dm), F32), _sds((T, dm), MX), _sds((mod3.shape[0], SUBLANE, dm), F32),
                             _sds((SUBLANE, LANE), F32)])(a, w, x2, tgt, mod3)


def _norm_bwd_tail(dhv, x_ref, dxi_ref, f_ref, mod_ref, n_ref, dx_ref, df_ref, st_ref, *, first, sub, coef):
    x = x_ref[...]
    r = _rsq(x)
    xhat = x * r
    n = n_ref[...]
    d_shift = jnp.sum(dhv, axis=0, keepdims=True)
    d_scale = jnp.sum(dhv * (xhat * n), axis=0, keepdims=True)
    dxn = dhv * (1.0 + mod_ref[0, 3 * sub + 1:3 * sub + 2, :])
    d_gain = jnp.sum(dxn * xhat, axis=0, keepdims=True)
    dxhat = dxn * n
    dx = dxi_ref[...] + r * (dxhat - xhat * jnp.mean(dxhat * xhat, axis=-1, keepdims=True))
    dx_ref[...] = dx

    @pl.when(first)
    def _():
        st_ref[...] = jnp.zeros_like(st_ref)

    st_ref[0, 0:1, :] += d_shift
    st_ref[0, 1:2, :] += d_scale
    st_ref[0, 2:3, :] += d_gain
    if f_ref is not None:
        st_ref[0, 3:4, :] += jnp.sum(coef * dx * f_ref[...], axis=0, keepdims=True)
        df_ref[...] = (coef * mod_ref[0, 3 * sub - 1:3 * sub, :] * dx).astype(df_ref.dtype)


def _ffn_dh_norm(dgu, w_st, x_cur, dx_in, f_prev, mod3, gain, *, sub, coef, name, tm=512, comm=None):
    T = dgu.shape[0]
    _, dm, hw = w_st.shape
    tm = _pick(T, tm)
    nb = mod3.shape[0]
    tps = (T // nb) // tm

    def body(d_ref, w_ref, x_ref, dxi_ref, f_ref, mod_ref, n_ref, dx_ref, df_ref, st_ref, acc_ref):
        i = pl.program_id(0)
        q = pl.program_id(1)
        part = lax.dot_general(d_ref[...], w_ref[0], (((1,), (1,)), ((), ())), preferred_element_type=F32)

        @pl.when(q == 0)
        def _():
            acc_ref[...] = part

        @pl.when(q > 0)
        def _():
            acc_ref[...] += part

        @pl.when(q == 3)
        def _():
            _norm_bwd_tail(acc_ref[...], x_ref, dxi_ref, f_ref, mod_ref, n_ref, dx_ref, df_ref, st_ref,
                           first=i % tps == 0, sub=sub, coef=coef)

    row = pl.BlockSpec((tm, dm), lambda i, q: (i, 0))
    return _pcall(body, name=name, grid=(T // tm, 4),
                  in_specs=[pl.BlockSpec((tm, hw), lambda i, q: (i, q)), pl.BlockSpec((1, dm, hw), lambda i, q: (_gu_shard(q), 0, 0)),
                            row, row, row, pl.BlockSpec((1, 9, dm), lambda i, q: (i // tps, 0, 0)),
                            pl.BlockSpec((1, dm), lambda i, q: (0, 0))],
                  out_specs=[row, row, pl.BlockSpec((1, SUBLANE, dm), lambda i, q: (i // tps, 0, 0))],
                  out_shape=[_sds((T, dm), F32), _sds((T, dm), MX), _sds((nb, SUBLANE, dm), F32)],
                  scratch=[pltpu.VMEM((tm, dm), F32)], comm=comm)(dgu, w_st, x_cur, dx_in, f_prev, mod3, gain)


def _mm_nt_norm(a, b, x_cur, dx_in, f_prev, mod3, gain, *, sub, coef, name, tm=512, comm=None):
    T, k = a.shape
    dm = b.shape[0]
    tm = _pick(T, tm)
    nb = mod3.shape[0]
    tps = (T // nb) // tm

    def body(a_ref, b_ref, x_ref, dxi_ref, f_ref, mod_ref, n_ref, dx_ref, df_ref, st_ref):
        dh = lax.dot_general(a_ref[...], b_ref[...], (((1,), (1,)), ((), ())), preferred_element_type=F32)
        _norm_bwd_tail(dh, x_ref, dxi_ref, f_ref, mod_ref, n_ref, dx_ref, df_ref, st_ref,
                       first=pl.program_id(0) % tps == 0, sub=sub, coef=coef)

    row = _row_spec(tm, dm)
    return _pcall(body, name=name, grid=(T // tm,),
                  in_specs=[_row_spec(tm, k), pl.BlockSpec((dm, k), lambda i: (0, 0)), row, row, row,
                            pl.BlockSpec((1, 9, dm), lambda i: (i // tps, 0, 0)), pl.BlockSpec((1, dm), lambda i: (0, 0))],
                  out_specs=[row, row, pl.BlockSpec((1, SUBLANE, dm), lambda i: (i // tps, 0, 0))],
                  out_shape=[_sds((T, dm), F32), _sds((T, dm), MX), _sds((nb, SUBLANE, dm), F32)],
                  comm=comm)(a, b, x_cur, dx_in, f_prev, mod3, gain)


def _bwd_block(x_cur, dh, dx_in, f_prev, mod3, gain, *, sub, coef, name, tm=256, comm=None):
    T, dm = x_cur.shape
    nb = mod3.shape[0]
    tps = (T // nb) // tm
    has_f = f_prev is not None
    mod_spec = pl.BlockSpec((1, 9, dm), lambda i: (i // tps, 0, 0))
    vec_spec = pl.BlockSpec((1, dm), lambda i: (0, 0))
    stat_spec = pl.BlockSpec((1, SUBLANE, dm), lambda i: (i // tps, 0, 0))
    row = _row_spec(tm, dm)

    def body(*refs):
        if has_f:
            x_ref, dh_ref, dxi_ref, f_ref, mod_ref, n_ref, dx_ref, df_ref, st_ref = refs
        else:
            x_ref, dh_ref, dxi_ref, mod_ref, n_ref, dx_ref, st_ref = refs
            f_ref = df_ref = None
        _norm_bwd_tail(dh_ref[...], x_ref, dxi_ref, f_ref, mod_ref, n_ref, dx_ref, df_ref, st_ref,
                       first=pl.program_id(0) % tps == 0, sub=sub, coef=coef)

    st_shape = _sds((nb, SUBLANE, dm), F32)
    if has_f:
        return _pcall(body, name=name, grid=(T // tm,), in_specs=[row, row, row, row, mod_spec, vec_spec],
                      out_specs=[row, row, stat_spec],
                      out_shape=[_sds((T, dm), F32), _sds((T, dm), MX), st_shape], comm=comm)(x_cur, dh, dx_in, f_prev, mod3, gain)
    return _pcall(body, name=name, grid=(T // tm,), in_specs=[row, row, row, mod_spec, vec_spec],
                  out_specs=[row, stat_spec], out_shape=[_sds((T, dm), F32), st_shape], comm=comm)(x_cur, dh, dx_in, mod3, gain)


def _mix_out_fwd(z, br_pool, attn, w_mla, w_out, x_prev, mod3, gain, *, sub, name, tm=512):
    T = z.shape[0]
    dm = w_out.shape[1]
    tm = _pick(T, tm)
    tps = (T // mod3.shape[0]) // tm
    whole = lambda a: pl.BlockSpec(a.shape, lambda i: (0, 0))

    def body(z_ref, bp_ref, at_ref, wm_ref, wo_ref, x_ref, mod_ref, n_ref, bm_ref, mg_ref, mo_ref, xo_ref, h_ref):
        bm = jnp.dot(at_ref[...], wm_ref[...], preferred_element_type=F32).astype(MX)
        bm_ref[...] = bm
        merged = (_sigmoid(z_ref[:, Z_GP:Z_GP + D]) * bp_ref[...] + _sigmoid(z_ref[:, Z_GM:Z_GM + D]) * bm).astype(MX)
        mg_ref[...] = merged
        mo = jnp.dot(merged, wo_ref[...], preferred_element_type=F32)
        mo_ref[...] = mo
        x = x_ref[...] + mod_ref[0, 3 * sub - 1:3 * sub, :] * mo
        xo_ref[...] = x
        xn = x * _rsq(x) * n_ref[...]
        h_ref[...] = (xn * (1.0 + mod_ref[0, 3 * sub + 1:3 * sub + 2, :]) + mod_ref[0, 3 * sub:3 * sub + 1, :]).astype(h_ref.dtype)

    row = _row_spec(tm, dm)
    return _pcall(body, name=name, grid=(T // tm,),
                  in_specs=[_row_spec(tm, Z_W), row, _row_spec(tm, attn.shape[1]), whole(w_mla), whole(w_out), row,
                            pl.BlockSpec((1, 9, dm), lambda i: (i // tps, 0, 0)), pl.BlockSpec((1, dm), lambda i: (0, 0))],
                  out_specs=[row, row, row, row, row],
                  out_shape=[_sds((T, dm), MX), _sds((T, dm), MX), _sds((T, dm), F32), _sds((T, dm), F32), _sds((T, dm), MX)],
                  )(z, br_pool, attn, w_mla, w_out, x_prev, mod3, gain)


def _mix_out_bwd(dmo, w_out, z, br_pool, br_mla, w_mla, *, name, tm=512):
    T = z.shape[0]
    tm = _pick(T, tm)
    whole = lambda a: pl.BlockSpec(a.shape, lambda i: (0, 0))
    nt = (((1,), (1,)), ((), ()))

    def body(dmo_ref, wo_ref, z_ref, bp_ref, bm_ref, wm_ref, dbp_ref, dbm_ref, dg_ref, dat_ref):
        dm = lax.dot_general(dmo_ref[...], wo_ref[...], nt, preferred_element_type=F32)
        sp = _sigmoid(z_ref[:, Z_GP:Z_GP + D])
        sm = _sigmoid(z_ref[:, Z_GM:Z_GM + D])
        dbp_ref[...] = (dm * sp).astype(dbp_ref.dtype)
        dbm = (dm * sm).astype(MX)
        dbm_ref[...] = dbm
        dg_ref[:, :D] = (dm * bp_ref[...].astype(F32) * sp * (1.0 - sp)).astype(dg_ref.dtype)
        dg_ref[:, D:] = (dm * bm_ref[...].astype(F32) * sm * (1.0 - sm)).astype(dg_ref.dtype)
        dat_ref[...] = lax.dot_general(dbm, wm_ref[...], nt, preferred_element_type=F32)

    row = _row_spec(tm, D)
    return _pcall(body, name=name, grid=(T // tm,),
                  in_specs=[row, whole(w_out), _row_spec(tm, Z_W), row, row, whole(w_mla)],
                  out_specs=[row, row, _row_spec(tm, 2 * D), _row_spec(tm, w_mla.shape[0])],
                  out_shape=[_sds((T, D), MX), _sds((T, D), MX), _sds((T, 2 * D), MX), _sds((T, w_mla.shape[0]), F32)],
                  )(dmo, w_out, z, br_pool, br_mla, w_mla)


def _shift_down(x, k, row):
    return jnp.where(row >= k, pltpu.roll(x, k, 0), 0.0)


def _shift_up(x, k, row, n):
    return jnp.where(row < n - k, pltpu.roll(x, n - k, 0), 0.0)


def _pool_fwd(z, pool_grp, pool_scale, *, nb, name):
    T = z.shape[0]
    S = T // nb
    blk = pl.BlockSpec((S, LANE), lambda b, g: (b, g))

    def body(u_ref, w_ref, s_ref, pooled_ref, mixed_ref, scaled_ref):
        g = pl.program_id(1)
        u = u_ref[...]
        row = lax.broadcasted_iota(jnp.int32, u.shape, 0)
        s2 = u + _shift_down(u, 1, row)
        s4 = s2 + _shift_down(s2, 2, row)
        s8 = s4 + _shift_down(s4, 4, row)
        s16 = s8 + _shift_down(s8, 8, row)
        win = jnp.where(g == 0, s2, jnp.where(g == 1, s4, jnp.where(g == 2, s8, s16)))
        width = lax.shift_left(jnp.int32(2), g)
        cnt = jnp.minimum(row + 1, width).astype(F32)
        pooled = (win / cnt - u).astype(MX)
        pooled_ref[...] = pooled
        mixed = jnp.dot(pooled, w_ref[0], preferred_element_type=F32)
        mixed_ref[...] = mixed
        scaled_ref[...] = (mixed * s_ref[...]).astype(scaled_ref.dtype)

    return _pcall(body, name=name, grid=(nb, POOL_G),
                  in_specs=[blk, pl.BlockSpec((1, LANE, LANE), lambda b, g: (g, 0, 0)),
                            pl.BlockSpec((1, LANE), lambda b, g: (0, g))],
                  out_specs=[blk, blk, blk],
                  out_shape=[_sds((T, POOL_W), MX), _sds((T, POOL_W), F32), _sds((T, POOL_W), MX)])(z, pool_grp, pool_scale)


def _pool_bwd(dscaled, mixed, pooled, pool_grp, pool_scale, *, nb, name):
    T = dscaled.shape[0]
    S = T // nb
    blk = pl.BlockSpec((S, LANE), lambda g, b: (b, g))

    def body(ds_ref, mixed_ref, pooled_ref, w_ref, s_ref, du_ref, dw_ref, dsc_ref):
        g = pl.program_id(0)
        b = pl.program_id(1)
        ds = ds_ref[...]
        dsc_ref[0] = jnp.sum(ds * mixed_ref[...], axis=0, keepdims=True)
        dmixed = (ds * s_ref[...]).astype(MX)
        dw = lax.dot_general(pooled_ref[...], dmixed, (((0,), (0,)), ((), ())), preferred_element_type=F32)

        @pl.when(b == 0)
        def _():
            dw_ref[0] = dw

        @pl.when(b > 0)
        def _():
            dw_ref[0] += dw
        dpooled = lax.dot_general(dmixed, w_ref[0], (((1,), (1,)), ((), ())), preferred_element_type=F32)
        row = lax.broadcasted_iota(jnp.int32, dpooled.shape, 0)
        width = lax.shift_left(jnp.int32(2), g)
        q = dpooled / jnp.minimum(row + 1, width).astype(F32)
        r2 = q + _shift_up(q, 1, row, S)
        r4 = r2 + _shift_up(r2, 2, row, S)
        r8 = r4 + _shift_up(r4, 4, row, S)
        r16 = r8 + _shift_up(r8, 8, row, S)
        win = jnp.where(g == 0, r2, jnp.where(g == 1, r4, jnp.where(g == 2, r8, r16)))
        du_ref[...] = (win - dpooled).astype(du_ref.dtype)

    return _pcall(body, name=name, grid=(POOL_G, nb),
                  in_specs=[blk, blk, blk, pl.BlockSpec((1, LANE, LANE), lambda g, b: (g, 0, 0)),
                            pl.BlockSpec((1, LANE), lambda g, b: (0, g))],
                  out_specs=[blk, pl.BlockSpec((1, LANE, LANE), lambda g, b: (g, 0, 0)),
                             pl.BlockSpec((1, 1, LANE), lambda g, b: (b, 0, g))],
                  out_shape=[_sds((T, POOL_W), MX), _sds((POOL_G, LANE, LANE), F32), _sds((nb, 1, POOL_W), F32)],
                  )(dscaled, mixed, pooled, pool_grp, pool_scale)


def _lane_masks(shape):
    lane = lax.broadcasted_iota(jnp.int32, shape, len(shape) - 1)
    m_n = lane < NOPE
    m_r = jnp.logical_and(lane >= KR_LANE, lane < KR_LANE + ROPE)
    first_half = lane < KR_LANE + ROPE // 2
    return m_n, m_r, first_half


def _rot(y, first_half):
    return jnp.where(first_half, -pltpu.roll(y, LANE - ROPE // 2, 1), pltpu.roll(y, ROPE // 2, 1))


def _rot_t(v, first_half, m_r):
    return jnp.where(m_r, jnp.where(first_half, pltpu.roll(v, LANE - ROPE // 2, 1), -pltpu.roll(v, ROPE // 2, 1)), 0.0)


def _mla_in_fwd(z, w_q, w_kv, qa_gain, kva_gain, cos, sin, q_gain, k_gain, *, name, tm=256):
    T = z.shape[0]
    slab = pl.BlockSpec((tm, LANE), lambda i: (i, 0))
    vec = pl.BlockSpec((1, LANE), lambda i: (0, 0))
    whole = lambda a: pl.BlockSpec(a.shape, lambda i: (0, 0))

    def body(z_ref, wq_ref, wkv_ref, qa_ref, kva_ref, cos_ref, sin_ref, qg_ref, kg_ref,
             qn_ref, kvn_ref, qp_ref, kvp_ref, q_ref, k_ref, v_ref):
        ql = z_ref[:, Z_QL:Z_QL + QL]
        kvl = z_ref[:, Z_KV:Z_KV + KVL]
        qn = (ql * _rsq(ql) * qa_ref[...]).astype(MX)
        kvn = (kvl * _rsq(kvl) * kva_ref[...]).astype(MX)
        qn_ref[...] = qn
        kvn_ref[...] = kvn
        qp = jnp.dot(qn, wq_ref[...], preferred_element_type=F32)
        kvp = jnp.dot(kvn, wkv_ref[...], preferred_element_type=F32)
        qp_ref[...] = qp
        kvp_ref[...] = kvp
        m_n, m_r, first_half = _lane_masks((tm, LANE))
        c = cos_ref[...]
        s = sin_ref[...]
        qg = qg_ref[...]
        kg = kg_ref[...]
        xr = z_ref[:, Z_KR:Z_KR + LANE]
        rr = lax.rsqrt(jnp.sum(xr * xr, axis=-1, keepdims=True) * (1.0 / ROPE) + EPS)
        yr = xr * rr * kg
        kr = jnp.where(m_r, yr * c + _rot(yr, first_half) * s, 0.0)
        for h in range(NH):
            x = qp[:, h * LANE:(h + 1) * LANE]
            x2 = x * x
            rn = lax.rsqrt(jnp.sum(jnp.where(m_n, x2, 0.0), axis=-1, keepdims=True) * (1.0 / NOPE) + EPS)
            rq = lax.rsqrt(jnp.sum(jnp.where(m_r, x2, 0.0), axis=-1, keepdims=True) * (1.0 / ROPE) + EPS)
            y = x * jnp.where(m_n, rn, jnp.where(m_r, rq, 0.0)) * qg
            q_ref[:, h * LANE:(h + 1) * LANE] = (y * c + _rot(y, first_half) * s).astype(q_ref.dtype)
            xk = kvp[:, h * LANE:(h + 1) * LANE]
            rk = lax.rsqrt(jnp.sum(jnp.where(m_n, xk * xk, 0.0), axis=-1, keepdims=True) * (1.0 / NOPE) + EPS)
            k_ref[:, h * LANE:(h + 1) * LANE] = (jnp.where(m_n, xk * rk * kg, 0.0) + kr).astype(k_ref.dtype)
        v_ref[...] = kvp[:, NH * LANE:].astype(v_ref.dtype)

    return _pcall(body, name=name, grid=(T // tm,),
                  in_specs=[_row_spec(tm, LAT_W), whole(w_q), whole(w_kv), whole(qa_gain), whole(kva_gain), slab, slab, vec, vec],
                  out_specs=[_row_spec(tm, QL), _row_spec(tm, KVL), _row_spec(tm, NH * LANE), _row_spec(tm, NH * LANE + NH * NOPE),
                             _row_spec(tm, NH * LANE), _row_spec(tm, NH * LANE), _row_spec(tm, NH * NOPE)],
                  out_shape=[_sds((T, QL), MX), _sds((T, KVL), MX), _sds((T, NH * LANE), F32), _sds((T, NH * LANE + NH * NOPE), F32),
                             _sds((T, NH * LANE), MX), _sds((T, NH * LANE), MX), _sds((T, NH * NOPE), MX)],
                  )(z, w_q, w_kv, qa_gain, kva_gain, cos, sin, q_gain, k_gain)


def _mla_in_bwd(dq, dk, dv, qp, kvp, z, w_q, w_kv, qa_gain, kva_gain, cos, sin, q_gain, k_gain, *, nb, name, tm=256):
    T = qp.shape[0]
    tps = (T // nb) // tm
    slab = pl.BlockSpec((tm, LANE), lambda i: (i, 0))
    vec = pl.BlockSpec((1, LANE), lambda i: (0, 0))
    whole = lambda a: pl.BlockSpec(a.shape, lambda i: (0, 0))

    def latent_bwd(x, dy, gain):
        r = _rsq(x)
        xhat = x * r
        dxhat = dy * gain
        return r * (dxhat - xhat * jnp.mean(dxhat * xhat, axis=-1, keepdims=True)), jnp.sum(dy * xhat, axis=0, keepdims=True)

    def body(dq_ref, dk_ref, dv_ref, qp_ref, kvp_ref, z_ref, wq_ref, wkv_ref, qa_ref, kva_ref, cos_ref, sin_ref, qg_ref, kg_ref,
             dqp_ref, dkvp_ref, dkr_ref, dql_ref, dkvl_ref, st_ref, sq_ref, sk_ref):
        i = pl.program_id(0)
        m_n, m_r, first_half = _lane_masks((tm, LANE))
        c = cos_ref[...]
        s = sin_ref[...]
        qg = qg_ref[...]
        kg = kg_ref[...]
        dqg = jnp.zeros((1, LANE), F32)
        dkg = jnp.zeros((1, LANE), F32)
        dkr_sum = jnp.zeros((tm, LANE), F32)
        for h in range(NH):
            x = qp_ref[:, h * LANE:(h + 1) * LANE]
            x2 = x * x
            rn = lax.rsqrt(jnp.sum(jnp.where(m_n, x2, 0.0), axis=-1, keepdims=True) * (1.0 / NOPE) + EPS)
            rq = lax.rsqrt(jnp.sum(jnp.where(m_r, x2, 0.0), axis=-1, keepdims=True) * (1.0 / ROPE) + EPS)
            rfac = jnp.where(m_n, rn, jnp.where(m_r, rq, 0.0))
            xhat = x * rfac
            do = dq_ref[:, h * LANE:(h + 1) * LANE]
            dy = do * c + _rot_t(do * s, first_half, m_r)
            dqg = dqg + jnp.sum(dy * xhat, axis=0, keepdims=True)
            dxhat = dy * qg
            t = dxhat * xhat
            mean_n = jnp.sum(jnp.where(m_n, t, 0.0), axis=-1, keepdims=True) * (1.0 / NOPE)
            mean_r = jnp.sum(jnp.where(m_r, t, 0.0), axis=-1, keepdims=True) * (1.0 / ROPE)
            dqp_ref[:, h * LANE:(h + 1) * LANE] = (
                rfac * (dxhat - xhat * jnp.where(m_n, mean_n, jnp.where(m_r, mean_r, 0.0)))).astype(dqp_ref.dtype)

            xk = kvp_ref[:, h * LANE:(h + 1) * LANE]
            rk = lax.rsqrt(jnp.sum(jnp.where(m_n, xk * xk, 0.0), axis=-1, keepdims=True) * (1.0 / NOPE) + EPS)
            khat = jnp.where(m_n, xk * rk, 0.0)
            dko = dk_ref[:, h * LANE:(h + 1) * LANE]
            dkn = jnp.where(m_n, dko, 0.0)
            dkg = dkg + jnp.sum(dkn * khat, axis=0, keepdims=True)
            dkhat = dkn * kg
            mean_k = jnp.sum(dkhat * khat, axis=-1, keepdims=True) * (1.0 / NOPE)
            dkvp_ref[:, h * LANE:(h + 1) * LANE] = jnp.where(m_n, rk * (dkhat - khat * mean_k), 0.0).astype(dkvp_ref.dtype)
            dkr_sum = dkr_sum + jnp.where(m_r, dko, 0.0)
        dkvp_ref[:, NH * LANE:] = dv_ref[...].astype(dkvp_ref.dtype)

        xr = z_ref[:, Z_KR:Z_KR + LANE]
        rr = lax.rsqrt(jnp.sum(xr * xr, axis=-1, keepdims=True) * (1.0 / ROPE) + EPS)
        rhat = xr * rr
        dyr = dkr_sum * c + _rot_t(dkr_sum * s, first_half, m_r)
        dkg = dkg + jnp.sum(dyr * rhat, axis=0, keepdims=True)
        drhat = dyr * kg
        mean_kr = jnp.sum(drhat * rhat, axis=-1, keepdims=True) * (1.0 / ROPE)
        dkr_ref[...] = jnp.where(m_r, rr * (drhat - rhat * mean_kr), 0.0).astype(dkr_ref.dtype)

        nt = (((1,), (1,)), ((), ()))
        dqn = lax.dot_general(dqp_ref[...], wq_ref[...], nt, preferred_element_type=F32)
        dkvn = lax.dot_general(dkvp_ref[...], wkv_ref[...], nt, preferred_element_type=F32)
        dql, dqa = latent_bwd(z_ref[:, Z_QL:Z_QL + QL], dqn, qa_ref[...])
        dkvl, dkva = latent_bwd(z_ref[:, Z_KV:Z_KV + KVL], dkvn, kva_ref[...])
        dql_ref[...] = dql.astype(dql_ref.dtype)
        dkvl_ref[...] = dkvl.astype(dkvl_ref.dtype)

        @pl.when(i % tps == 0)
        def _():
            st_ref[...] = jnp.zeros_like(st_ref)
            sq_ref[...] = jnp.zeros_like(sq_ref)
            sk_ref[...] = jnp.zeros_like(sk_ref)

        st_ref[0, 0:1, :] += dqg
        st_ref[0, 1:2, :] += dkg
        sq_ref[0, 0:1, :] += dqa
        sk_ref[0, 0:1, :] += dkva

    stat = lambda w: pl.BlockSpec((1, SUBLANE, w), lambda i: (i // tps, 0, 0))
    return _pcall(body, name=name, grid=(T // tm,),
                  in_specs=[_row_spec(tm, NH * LANE), _row_spec(tm, NH * LANE), _row_spec(tm, NH * NOPE),
                            _row_spec(tm, NH * LANE), _row_spec(tm, NH * LANE + NH * NOPE), _row_spec(tm, LAT_W),
                            whole(w_q), whole(w_kv), whole(qa_gain), whole(kva_gain), slab, slab, vec, vec],
                  out_specs=[_row_spec(tm, NH * LANE), _row_spec(tm, NH * LANE + NH * NOPE), slab, _row_spec(tm, QL),
                             _row_spec(tm, KVL), stat(LANE), stat(QL), stat(KVL)],
                  out_shape=[_sds((T, NH * LANE), MX), _sds((T, NH * LANE + NH * NOPE), MX), _sds((T, LANE), MX),
                             _sds((T, QL), MX), _sds((T, KVL), MX), _sds((nb, SUBLANE, LANE), F32),
                             _sds((nb, SUBLANE, QL), F32), _sds((nb, SUBLANE, KVL), F32)],
                  )(dq, dk, dv, qp, kvp, z, w_q, w_kv, qa_gain, kva_gain, cos, sin, q_gain, k_gain)


def _lower_triangle(t):
    return lax.broadcasted_iota(jnp.int32, (t, t), 1) <= lax.broadcasted_iota(jnp.int32, (t, t), 0)


def _attn_fwd(q, k, v, *, nb, name, tq=512, comm=None):
    T = q.shape[0]
    S = T // nb
    tq = _pick(S, tq)
    nq = S // tq
    tk = tq
    npair = NH // 2

    def body(q_ref, k_ref, v_ref, o_ref, lse_ref):
        qi = pl.program_id(2)
        lane = lax.broadcasted_iota(jnp.int32, (tq, LANE), 1)
        qs = [q_ref[:, hh * LANE:(hh + 1) * LANE] for hh in range(2)]

        def block(j, carry, diagonal):
            k0 = pl.multiple_of(j * tk, tk)
            vb = v_ref[pl.ds(k0, tk), :]
            new = []
            for hh in range(2):
                m, l, acc = carry[hh]
                kb = k_ref[pl.ds(k0, tk), hh * LANE:(hh + 1) * LANE]
                s = lax.dot_general(qs[hh], kb, (((1,), (1,)), ((), ())), preferred_element_type=F32) * ATTN_SCALE
                if diagonal:
                    s = jnp.where(_lower_triangle(tq), s, NEG)
                m_new = jnp.maximum(m, jnp.max(s, axis=-1, keepdims=True))
                p = jnp.exp(s - m_new)
                alpha = jnp.exp(m - m_new)
                l = alpha * l + jnp.sum(p, axis=-1, keepdims=True)
                acc = alpha * acc + jnp.dot(p.astype(MX), vb, preferred_element_type=F32)
                new.append((m_new, l, acc))
            return tuple(new)

        init = tuple((jnp.full((tq, 1), NEG, F32), jnp.zeros((tq, 1), F32), jnp.zeros((tq, LANE), F32)) for _ in range(2))
        carry = lax.fori_loop(0, qi, lambda j, c: block(j, c, False), init)
        (m0, l0, acc0), (m1, l1, acc1) = block(qi, carry, True)
        o_ref[...] = jnp.where(lane < NOPE, acc0 / l0, acc1 / l1).astype(o_ref.dtype)
        lse_ref[...] = jnp.where(lane < NOPE, m0 + jnp.log(l0), m1 + jnp.log(l1))

    return _pcall(body, name=name, grid=(nb, npair, nq),
                  in_specs=[pl.BlockSpec((tq, 2 * LANE), lambda b, p, i: (b * nq + i, p)),
                            pl.BlockSpec((S, 2 * LANE), lambda b, p, i: (b, p)),
                            pl.BlockSpec((S, LANE), lambda b, p, i: (b, p))],
                  out_specs=[pl.BlockSpec((tq, LANE), lambda b, p, i: (b * nq + i, p)),
                             pl.BlockSpec((tq, LANE), lambda b, p, i: (b * nq + i, p))],
                  out_shape=[_sds((T, NH * NOPE), MX), _sds((T, NH * NOPE), F32)], comm=comm)(q, k, v)


def _attn_bwd(q, k, v, o, lse, do, *, nb, name, tq=512, comm=None):
    T = q.shape[0]
    S = T // nb
    tq = _pick(S, tq)
    nq = S // tq
    tk = tq
    npair = NH // 2

    def body(q_ref, k_ref, v_ref, o_ref, lse_ref, do_ref, dq_ref, dk_ref, dv_ref, delta_ref):
        lane = lax.broadcasted_iota(jnp.int32, (tq, LANE), 1)
        first = lane < NOPE
        dq_ref[...] = jnp.zeros_like(dq_ref)

        def delta_step(qi, _):
            q0 = pl.multiple_of(qi * tq, tq)
            prod = do_ref[pl.ds(q0, tq), :] * o_ref[pl.ds(q0, tq), :].astype(F32)
            d0 = jnp.sum(jnp.where(first, prod, 0.0), axis=-1, keepdims=True)
            d1 = jnp.sum(jnp.where(first, 0.0, prod), axis=-1, keepdims=True)
            delta_ref[pl.ds(q0, tq), :] = jnp.where(first, d0, d1)
            return 0

        lax.fori_loop(0, nq, delta_step, 0)

        def kv_step(kj, _):
            k0 = pl.multiple_of(kj * tk, tk)
            kbs = [k_ref[pl.ds(k0, tk), hh * LANE:(hh + 1) * LANE] for hh in range(2)]
            vb = v_ref[pl.ds(k0, tk), :]

            def q_block(qi, carry, diagonal):
                dk0, dk1, dv = carry
                dks = [dk0, dk1]
                q0 = pl.multiple_of(qi * tq, tq)
                dov = do_ref[pl.ds(q0, tq), :]
                lse_v = lse_ref[pl.ds(q0, tq), :]
                delta_v = delta_ref[pl.ds(q0, tq), :]
                for hh in range(2):
                    qh = q_ref[pl.ds(q0, tq), hh * LANE:(hh + 1) * LANE]
                    dob = jnp.where(first if hh == 0 else jnp.logical_not(first), dov, 0.0).astype(MX)
                    s = lax.dot_general(qh, kbs[hh], (((1,), (1,)), ((), ())), preferred_element_type=F32) * ATTN_SCALE
                    p = jnp.exp(s - lse_v[:, hh * NOPE:hh * NOPE + 1])
                    if diagonal:
                        p = jnp.where(_lower_triangle(tq), p, 0.0)
                    dp = lax.dot_general(dob, vb, (((1,), (1,)), ((), ())), preferred_element_type=F32)
                    ds = (p * (dp - delta_v[:, hh * NOPE:hh * NOPE + 1]) * ATTN_SCALE).astype(MX)
                    dks[hh] = dks[hh] + lax.dot_general(ds, qh, (((0,), (0,)), ((), ())), preferred_element_type=F32)
                    dv = dv + lax.dot_general(p.astype(MX), dob, (((0,), (0,)), ((), ())), preferred_element_type=F32)
                    dq_ref[pl.ds(q0, tq), hh * LANE:(hh + 1) * LANE] += jnp.dot(ds, kbs[hh], preferred_element_type=F32)
                return dks[0], dks[1], dv

            zero = jnp.zeros((tk, LANE), F32)
            carry = q_block(kj, (zero, zero, zero), True)
            dk0, dk1, dv = lax.fori_loop(kj + 1, nq, lambda qi, c: q_block(qi, c, False), carry)
            dk_ref[pl.ds(k0, tk), 0:LANE] = dk0
            dk_ref[pl.ds(k0, tk), LANE:2 * LANE] = dk1
            dv_ref[pl.ds(k0, tk), :] = dv
            return 0

        lax.fori_loop(0, nq, kv_step, 0)

    pair256 = pl.BlockSpec((S, 2 * LANE), lambda b, p: (b, p))
    pair128 = pl.BlockSpec((S, LANE), lambda b, p: (b, p))
    return _pcall(body, name=name, grid=(nb, npair),
                  in_specs=[pair256, pair256, pair128, pair128, pair128, pair128],
                  out_specs=[pair256, pair256, pair128],
                  out_shape=[_sds((T, NH * LANE), F32), _sds((T, NH * LANE), F32), _sds((T, NH * NOPE), F32)],
                  scratch=[pltpu.VMEM((S, LANE), F32)], comm=comm)(q, k, v, o, lse, do)


def _run(plan, fn, *args, name, **kw):
    rider = plan.rider(name)
    if rider is None:
        return fn(*args, name=name, **kw)
    outs, landed = fn(*args, name=name, comm=rider, **kw)
    plan.landed(name, landed)
    return outs


def _layer_fwd_bwd(x, tgt, mod3, cos, sin, plan, P):
    nb = mod3.shape[0]
    W = plan.W
    h1, gu1, a1 = _run(plan, _ffn_up_first, x, mod3, P["norm_ffn1"], W["ffn1_in"], sub=0, name="ffn1_up_a")
    gu1, a1 = _run(plan, _ffn_up_second, h1, W["ffn1_in"], gu1, a1, name="ffn1_up_b")
    f1, x1, h2 = _run(plan, _mm_resid_norm, a1, W["ffn1_out"], x, mod3, P["norm_mix"], sub=1, coef=0.5, name="ffn1_out")
    z = _run(plan, _mm, h2, W["w_in"], mode="nn", out_dtype=F32, name="mix_in", tn=1664)
    pooled, mixed, scaled = _pool_fwd(z, W["pool_grp"], P["pool_scale"], nb=nb, name="pool_fwd")
    br_pool = _mm(scaled, W["pool_proj"], mode="nn", out_dtype=MX, name="pool_proj")
    qn, kvn, qp, kvp, q, k, v = _mla_in_fwd(z, W["q_up"], W["kv_up"], P["q_a_norm"], P["kv_a_norm"], cos, sin,
                                            P["q_gain"], P["k_gain"], name="mla_in_fwd")
    attn, lse = _run(plan, _attn_fwd, q, k, v, nb=nb, name="attn_fwd")
    br_mla, merged, mo, x2, h3 = _mix_out_fwd(z, br_pool, attn, W["mla_proj"], W["w_out"], x1, mod3, P["norm_ffn2"], sub=2,
                                              name="mix_out", tm=256)
    gu2, a2 = _ffn_up(h3, W["ffn2_in"], name="ffn2_up")
    dy, df2, st_fin, loss = _mm_loss(a2, W["ffn2_out"], x2, tgt, mod3, name="ffn2_out_loss")

    plan.grad("ffn2_out", _dw(a2, df2, name="d_ffn2_out"))
    dgu2 = _ffn_dact(df2, gu2, W["ffn2_out"], name="ffn2_dact")
    plan.grad("ffn2_in", _ffn_dw_in(h3, dgu2, name="d_ffn2_in"))
    dx2, dmo, st3 = _run(plan, _ffn_dh_norm, dgu2, W["ffn2_in"], x2, dy, mo, mod3, P["norm_ffn2"], sub=2, coef=1.0,
                         name="d_ffn2_h")
    plan.grad("w_out", _dw(merged, dmo, name="d_mix_out"))
    dbr_pool, dbr_mla, dgates, dattn = _mix_out_bwd(dmo, W["w_out"], z, br_pool, br_mla, W["mla_proj"], name="mix_out_bwd", tm=256)
    plan.grad("pool_proj", _dw(scaled, dbr_pool, name="d_pool_proj"))
    dscaled = _mm(dbr_pool, W["pool_proj"], mode="nt", out_dtype=F32, name="d_pool_scaled")
    du_pool, d_pool_grp, d_pool_scale = _pool_bwd(dscaled, mixed, pooled, W["pool_grp"], P["pool_scale"], nb=nb, name="pool_bwd")
    plan.grad("mla_proj", _dw(attn, dbr_mla, name="d_mla_proj"))
    dq, dk, dv = _run(plan, _attn_bwd, q, k, v, attn, lse, dattn, nb=nb, name="attn_bwd")
    dqp, dkvp, dkr, dql, dkvl, st_prep, st_q, st_kv = _mla_in_bwd(
        dq, dk, dv, qp, kvp, z, W["q_up"], W["kv_up"], P["q_a_norm"], P["kv_a_norm"], cos, sin, P["q_gain"], P["k_gain"],
        nb=nb, name="mla_in_bwd")
    plan.grad("q_up", _dw(qn, dqp, name="d_q_up"))
    plan.grad("kv_up", _dw(kvn, dkvp, name="d_kv_up"))
    dz = jnp.concatenate([du_pool, dql, dkvl, dkr, dgates], axis=1)
    plan.grad("w_in", _run(plan, _dw, h2, dz, name="d_mix_in", tm=256, tn=1664))
    dx1, df1, st2 = _run(plan, _mm_nt_norm, dz, W["w_in"], x1, dx2, f1, mod3, P["norm_mix"], sub=1, coef=0.5, name="d_mix_h")
    plan.grad("ffn1_out", _run(plan, _dw, a1, df1, name="d_ffn1_out"))
    dgu1 = _run(plan, _ffn_dact, df1, gu1, W["ffn1_out"], name="ffn1_dact")
    half = x.shape[1] // 2
    plan.grad("ffn1_in@0", _run(plan, _ffn_dw_in, h1, dgu1, rows=(0, half), name="d_ffn1_in_a"))
    plan.grad("ffn1_in@1", _run(plan, _ffn_dw_in, h1, dgu1, rows=(half, half), name="d_ffn1_in_b"))
    dh1 = _run(plan, _ffn_dh, dgu1, W["ffn1_in"], name="d_ffn1_h")
    grad_x, st1 = _run(plan, _bwd_block, x, dh1, dx1, None, mod3, P["norm_ffn1"], sub=0, coef=0.0, name="bwd_norm1")

    return loss, grad_x, (st1, st2, st3, st_fin, st_q, st_kv, st_prep, d_pool_scale), d_pool_grp


def _w_in_to_kernel(w):
    k = w.shape[0]
    zeros = lambda n: jnp.zeros((k, n), w.dtype)
    return jnp.concatenate([w[:, 0:1152], zeros(KR_LANE), w[:, 1152:1184], zeros(LANE - KR_LANE - ROPE), w[:, 1184:]], axis=1)


def _w_in_from_kernel(g):
    return jnp.concatenate([g[:, 0:1152], g[:, Z_KR + KR_LANE:Z_KR + KR_LANE + ROPE], g[:, Z_GP:]], axis=1)


def _q_up_to_kernel(w):
    k = w.shape[0]
    return jnp.pad(w.reshape(k, NH, NOPE + ROPE), ((0, 0), (0, 0), (0, LANE - NOPE - ROPE))).reshape(k, NH * LANE)


def _q_up_from_kernel(g):
    k = g.shape[0]
    return g.reshape(k, NH, LANE)[:, :, :NOPE + ROPE].reshape(k, NH * (NOPE + ROPE))


def _kv_up_to_kernel(w):
    k = w.shape[0]
    w3 = w.reshape(k, NH, 2 * NOPE)
    kpart = jnp.pad(w3[:, :, :NOPE], ((0, 0), (0, 0), (0, LANE - NOPE))).reshape(k, NH * LANE)
    return jnp.concatenate([kpart, w3[:, :, NOPE:].reshape(k, NH * NOPE)], axis=1)


def _kv_up_from_kernel(g):
    k = g.shape[0]
    kpart = g[:, :NH * LANE].reshape(k, NH, LANE)[:, :, :NOPE]
    vpart = g[:, NH * LANE:].reshape(k, NH, NOPE)
    return jnp.concatenate([kpart, vpart], axis=2).reshape(k, NH * 2 * NOPE)


def _gain_slab(nope, rope):
    return jnp.concatenate([nope, rope, jnp.zeros((1, LANE - NOPE - ROPE), nope.dtype)], axis=1)


def _rope_tables(positions):
    inv_freq = 10000.0 ** (-jnp.arange(0, ROPE, 2, dtype=F32) / ROPE)
    ang = positions.astype(F32)[:, None] * inv_freq
    ang = jnp.concatenate([ang, ang], axis=-1)
    t = positions.shape[0]
    cos = jnp.concatenate([jnp.ones((t, KR_LANE), F32), jnp.cos(ang), jnp.ones((t, LANE - KR_LANE - ROPE), F32)], axis=1)
    sin = jnp.concatenate([jnp.zeros((t, KR_LANE), F32), jnp.sin(ang), jnp.zeros((t, LANE - KR_LANE - ROPE), F32)], axis=1)
    return cos, sin


def _coords():
    return lax.axis_index("x"), lax.axis_index("y"), lax.axis_index("c")


HBM_SPEC = pl.BlockSpec(memory_space=pl.ANY)
VMEM_SPEC = pl.BlockSpec(memory_space=pltpu.VMEM)


CHIP_RELS = ((1, 0), (0, 1), (1, 1))


def _gather_comm(shards):
    n = len(shards)

    def ici(ins, outs, sems, a, j, x, y, cc):
        half = ins[a].shape[0] // 2
        mine = pl.ds(cc * half, half)
        dx, dy = CHIP_RELS[j]
        return pltpu.make_async_remote_copy(src_ref=ins[a].at[mine], dst_ref=outs[a].at[2 * x + y, mine],
                                            send_sem=sems[0].at[a, j], recv_sem=sems[1].at[a, j],
                                            device_id=(x ^ dx, y ^ dy, cc), device_id_type=MESH)

    def d2d(ins, outs, sems, a, j, x, y, cc, half_of):
        half = ins[a].shape[0] // 2
        dx, dy = CHIP_RELS[j]
        landed = outs[a].at[2 * (x ^ dx) + (y ^ dy), pl.ds(half_of * half, half)]
        return pltpu.make_async_remote_copy(src_ref=landed, dst_ref=landed, send_sem=sems[2].at[a, j], recv_sem=sems[3].at[a, j],
                                            device_id=(x, y, 1 - cc), device_id_type=MESH)

    def own(ins, outs, sems, a, x, y):
        return pltpu.make_async_copy(ins[a], outs[a].at[2 * x + y], sems[4].at[a])

    def start(ins, outs, sems):
        x, y, cc = _coords()
        for a in range(n):
            own(ins, outs, sems, a, x, y).start()
            for j in range(3):
                ici(ins, outs, sems, a, j, x, y, cc).start()

    def finish(ins, outs, sems):
        x, y, cc = _coords()
        for a in range(n):
            for j in range(3):
                ici(ins, outs, sems, a, j, x, y, cc).wait_recv()
                d2d(ins, outs, sems, a, j, x, y, cc, cc).start()
        for a in range(n):
            for j in range(3):
                d2d(ins, outs, sems, a, j, x, y, cc, 1 - cc).wait_recv()
        for a in range(n):
            for j in range(3):
                ici(ins, outs, sems, a, j, x, y, cc).wait_send()
                d2d(ins, outs, sems, a, j, x, y, cc, cc).wait_send()
            own(ins, outs, sems, a, x, y).wait()

    dma = pltpu.SemaphoreType.DMA
    return _Comm(shards, [_sds((N_CHIP,) + s.shape, s.dtype) for s in shards],
                 [dma((n, 3)), dma((n, 3)), dma((n, 3)), dma((n, 3)), dma((n,))], start, finish)


def _swap_comm(parts):
    n = len(parts)

    def copy(ins, outs, sems, a):
        x, y, cc = _coords()
        half = ins[a].shape[1] // 2
        return pltpu.make_async_remote_copy(src_ref=ins[a].at[:, pl.ds((1 - cc) * half, half)], dst_ref=outs[a],
                                            send_sem=sems[0].at[a], recv_sem=sems[1].at[a], device_id=(x, y, 1 - cc),
                                            device_id_type=MESH)

    def start(ins, outs, sems):
        for a in range(n):
            copy(ins, outs, sems, a).start()

    def finish(ins, outs, sems):
        for a in range(n):
            copy(ins, outs, sems, a).wait()

    dma = pltpu.SemaphoreType.DMA
    return _Comm(parts, [_sds((p.shape[0], p.shape[1] // 2, p.shape[2]), p.dtype) for p in parts], [dma((n,)), dma((n,))],
                 start, finish)


def _add_half(full, other, cidx, *, name):
    nch, r, c = full.shape
    half = r // 2
    tr = _pick_rows(half)
    nbk = half // tr
    grid_spec = pltpu.PrefetchScalarGridSpec(
        num_scalar_prefetch=1, grid=(nch, nbk),
        in_specs=[pl.BlockSpec((1, tr, c), lambda j, i, cref: (j, cref[0] * nbk + i, 0)),
                  pl.BlockSpec((1, tr, c), lambda j, i, cref: (j, i, 0))],
        out_specs=pl.BlockSpec((1, tr, c), lambda j, i, cref: (j, i, 0)))

    def body(cref, a_ref, b_ref, o_ref):
        o_ref[...] = (a_ref[...] + b_ref[...]).astype(o_ref.dtype)

    return _pcall(body, name=name, out_shape=_sds((nch, half, c), MX), grid_spec=grid_spec)(cidx, full, other)


def _pick_rows(rows, target=512):
    best = None
    for t in range(16, min(rows, target) + 1, 16):
        if rows % t == 0:
            best = t
    return rows if best is None else best


def _exchange_comm(parts):
    n = len(parts)

    def send(ins, outs, sems, a, j, x, y, cc):
        dx, dy = CHIP_RELS[j]
        return pltpu.make_async_remote_copy(src_ref=ins[a].at[2 * (x ^ dx) + (y ^ dy)], dst_ref=outs[a].at[2 * x + y],
                                            send_sem=sems[0].at[a, j], recv_sem=sems[1].at[a, j],
                                            device_id=(x ^ dx, y ^ dy, cc), device_id_type=MESH)

    def landing(ins, outs, sems, a, j, x, y, cc):
        dx, dy = CHIP_RELS[j]
        peer_chip = 2 * (x ^ dx) + (y ^ dy)
        return pltpu.make_async_remote_copy(src_ref=ins[a].at[peer_chip], dst_ref=outs[a].at[peer_chip], send_sem=sems[0].at[a, j],
                                            recv_sem=sems[1].at[a, j], device_id=(x, y, cc), device_id_type=MESH)

    def own(ins, outs, sems, a, x, y):
        return pltpu.make_async_copy(ins[a].at[2 * x + y], outs[a].at[2 * x + y], sems[2].at[a])

    def start(ins, outs, sems):
        x, y, cc = _coords()
        for a in range(n):
            own(ins, outs, sems, a, x, y).start()
            for j in range(3):
                send(ins, outs, sems, a, j, x, y, cc).start()

    def finish(ins, outs, sems):
        x, y, cc = _coords()
        for a in range(n):
            for j in range(3):
                landing(ins, outs, sems, a, j, x, y, cc).wait_recv()
        for a in range(n):
            for j in range(3):
                send(ins, outs, sems, a, j, x, y, cc).wait_send()
            own(ins, outs, sems, a, x, y).wait()

    dma = pltpu.SemaphoreType.DMA
    return _Comm(parts, [_sds(p.shape, p.dtype) for p in parts], [dma((n, 3)), dma((n, 3)), dma((n,))], start, finish)


def _sum_chips(q, cidx, *, name):
    nch, h, c = q.shape
    tr = _pick_rows(h)
    nbk = h // tr
    grid_spec = pltpu.PrefetchScalarGridSpec(
        num_scalar_prefetch=1, grid=(nbk,),
        in_specs=[pl.BlockSpec((nch, tr, c), lambda i, cref: (0, i, 0))],
        out_specs=pl.BlockSpec((tr, c), lambda i, cref: (cref[0] * nbk + i, 0)))

    def body(cref, q_ref, o_ref):
        acc = q_ref[0].astype(F32) + q_ref[1].astype(F32)
        acc = acc + q_ref[2].astype(F32)
        o_ref[...] = acc + q_ref[3].astype(F32)

    return _pcall(body, name=name, out_shape=_sds((2 * h, c), F32), grid_spec=grid_spec)(cidx, q)


def _join_comm(fulls):
    n = len(fulls)

    def half_copy(outs, sems, a, which):
        x, y, cc = _coords()
        h = outs[a].shape[0] // 2
        rows = outs[a].at[pl.ds((cc if which == 0 else 1 - cc) * h, h)]
        return pltpu.make_async_remote_copy(src_ref=rows, dst_ref=rows, send_sem=sems[0].at[a], recv_sem=sems[1].at[a],
                                            device_id=(x, y, 1 - cc), device_id_type=MESH)

    def start(ins, outs, sems):
        for a in range(n):
            half_copy(outs, sems, a, 0).start()

    def finish(ins, outs, sems):
        for a in range(n):
            half_copy(outs, sems, a, 1).wait_recv()
        for a in range(n):
            half_copy(outs, sems, a, 0).wait_send()

    dma = pltpu.SemaphoreType.DMA
    return _Comm(fulls, [_sds(p.shape, p.dtype) for p in fulls], [dma((n,)), dma((n,))], start, finish,
                 aliases={a: a for a in range(n)})


def _ada_prologue(c, w, b, *, name, comm=None):
    nb, dm = c.shape
    n = w.shape[1]

    def body(c_ref, w_ref, b_ref, call_ref, land_ref, cpad, mod_s, g_send, g_recv, m_send, m_recv):
        x, y, cc = _coords()
        me = 4 * x + 2 * y + cc
        chip = 2 * x + y
        cpad[...] = jnp.zeros_like(cpad)
        cpad[0:nb, :] = c_ref[...]
        copies = []
        for kk in range(1, N_DEV):
            peer = (x ^ (kk >> 2), y ^ ((kk >> 1) & 1), cc ^ (kk & 1))
            cp = pltpu.make_async_remote_copy(src_ref=cpad, dst_ref=call_ref.at[me], send_sem=g_send.at[kk - 1],
                                              recv_sem=g_recv.at[kk - 1], device_id=peer, device_id_type=MESH)
            cp.start()
            copies.append(cp)
        call_ref[me] = cpad[...]
        for kk in range(1, N_DEV):
            pltpu.make_async_remote_copy(src_ref=cpad, dst_ref=call_ref.at[me ^ kk], send_sem=g_send.at[kk - 1],
                                         recv_sem=g_recv.at[kk - 1], device_id=(x, y, cc), device_id_type=MESH).wait_recv()
        cv = call_ref[...].reshape(N_DEV * SUBLANE, dm)
        act = (cv * _sigmoid(cv)).astype(MX)
        mod = jnp.dot(act, w_ref[...].astype(MX), preferred_element_type=F32) + b_ref[...]
        mod_s[...] = mod.reshape(N_DEV, SUBLANE, n)
        for j, (dx, dy) in enumerate(CHIP_RELS):
            cp = pltpu.make_async_remote_copy(src_ref=mod_s.at[4 * (x ^ dx) + 2 * (y ^ dy) + cc], dst_ref=land_ref.at[chip],
                                              send_sem=m_send.at[j], recv_sem=m_recv.at[j],
                                              device_id=(x ^ dx, y ^ dy, cc), device_id_type=MESH)
            cp.start()
            copies.append(cp)
        land_ref[chip] = mod_s[me]
        for j, (dx, dy) in enumerate(CHIP_RELS):
            pltpu.make_async_remote_copy(src_ref=mod_s.at[me], dst_ref=land_ref.at[2 * (x ^ dx) + (y ^ dy)], send_sem=m_send.at[j],
                                         recv_sem=m_recv.at[j], device_id=(x, y, cc), device_id_type=MESH).wait_recv()
        for cp in copies:
            cp.wait_send()

    dma = pltpu.SemaphoreType.DMA
    return _pcall(body, name=name, in_specs=[VMEM_SPEC] * 3, out_specs=[VMEM_SPEC] * 2,
                  out_shape=[_sds((N_DEV, SUBLANE, dm), F32), _sds((N_CHIP, SUBLANE, n), F32)],
                  scratch=[pltpu.VMEM((SUBLANE, dm), F32), pltpu.VMEM((N_DEV, SUBLANE, n), F32),
                           dma((N_DEV - 1,)), dma((N_DEV - 1,)), dma((3,)), dma((3,))], comm=comm)(c, w, b)


def _ada_bwd(c_all, dmod_cols, *, name):
    m, kdim = c_all.shape
    n = dmod_cols.shape[1]
    tn = _pick(n, 1152)

    def body(c_ref, d_ref, o_ref):
        cv = c_ref[...]
        act = (cv * _sigmoid(cv)).astype(MX)
        o_ref[...] = lax.dot_general(act, d_ref[...].astype(MX), (((0,), (0,)), ((), ())), preferred_element_type=F32)

    return _pcall(body, name=name, out_shape=_sds((kdim, n), F32), grid=(n // tn,),
                  in_specs=[pl.BlockSpec((m, kdim), lambda j: (0, 0)), pl.BlockSpec((m, tn), lambda j: (0, j))],
                  out_specs=pl.BlockSpec((kdim, tn), lambda j: (0, j)))(c_all, dmod_cols)


SLAB_W = 1024
RED_ROWS = 8


LOSS_LANE = SLAB_W - LANE


def _pack_stats(st1, st2, st3, st_fin, st_q, st_kv, st_prep, d_pool_scale, loss8, *, name):
    nb = st1.shape[0]
    dm_rows = -(-9 * nb // SUBLANE) * SUBLANE

    def body(s1, s2, s3, sf, sq, skv, sp, sps, loss_ref, o_ref):
        o_ref[...] = jnp.zeros_like(o_ref)
        for s in range(nb):
            rows = [s1[s, 0:1, :], s1[s, 1:2, :], s2[s, 3:4, :], s2[s, 0:1, :], s2[s, 1:2, :], s3[s, 3:4, :], s3[s, 0:1, :],
                    s3[s, 1:2, :], sf[s, 0:1, :]]
            for k, row in enumerate(rows):
                o_ref[9 * s + k:9 * s + k + 1, :] = row

        def over_seq(ref, r):
            acc = ref[0, r:r + 1, :]
            for s in range(1, nb):
                acc = acc + ref[s, r:r + 1, :]
            return acc

        o_ref[dm_rows + 0:dm_rows + 1, :] = over_seq(s1, 2)
        o_ref[dm_rows + 1:dm_rows + 2, :] = over_seq(s2, 2)
        o_ref[dm_rows + 2:dm_rows + 3, :] = over_seq(s3, 2)
        o_ref[dm_rows + 3:dm_rows + 4, 0:POOL_W] = over_seq(sps, 0)
        o_ref[dm_rows + 4:dm_rows + 5, 0:QL] = over_seq(sq, 0)
        o_ref[dm_rows + 5:dm_rows + 6, 0:KVL] = over_seq(skv, 0)
        o_ref[dm_rows + 6:dm_rows + 7, 0:LANE] = over_seq(sp, 0)
        o_ref[dm_rows + 7:dm_rows + 8, 0:LANE] = over_seq(sp, 1)
        o_ref[dm_rows + 7:dm_rows + 8, LOSS_LANE:] = loss_ref[0:1, :]

    return _pcall(body, name=name, out_shape=_sds((dm_rows + RED_ROWS, SLAB_W), F32), in_specs=[VMEM_SPEC] * 9,
                  out_specs=VMEM_SPEC)(st1, st2, st3, st_fin, st_q, st_kv, st_prep, d_pool_scale, loss8)


def _small_allreduce(slab, pool, *, name, comm=None):
    rows, w = slab.shape
    dm = rows - RED_ROWS
    prow, pw = pool.shape
    crow = RED_ROWS + 2 * dm

    def body(slab_ref, pool_ref, red_ref, dmod_ref, ptot_ref, sib_slab, sib_pool, chip_slab, chip_pool, land_slab, land_pool,
             a_send, a_recv, b_send, b_recv):
        x, y, cc = _coords()
        chip = 2 * x + y
        sib = (x, y, 1 - cc)
        to_sib = [pltpu.make_async_remote_copy(src_ref=slab_ref, dst_ref=sib_slab, send_sem=a_send.at[0], recv_sem=a_recv.at[0],
                                               device_id=sib, device_id_type=MESH),
                  pltpu.make_async_remote_copy(src_ref=pool_ref, dst_ref=sib_pool, send_sem=a_send.at[1], recv_sem=a_recv.at[1],
                                               device_id=sib, device_id_type=MESH)]
        for cp in to_sib:
            cp.start()
        for cp in to_sib:
            cp.wait()
        mine_dm, theirs_dm = slab_ref[0:dm, :], sib_slab[0:dm, :]
        chip_slab[0:RED_ROWS, :] = slab_ref[dm:, :] + sib_slab[dm:, :]
        chip_slab[RED_ROWS:RED_ROWS + dm, :] = jnp.where(cc == 0, mine_dm, theirs_dm)
        chip_slab[RED_ROWS + dm:, :] = jnp.where(cc == 0, theirs_dm, mine_dm)
        chip_pool[...] = pool_ref[...] + sib_pool[...]

        sends = []
        for j, (dx, dy) in enumerate(CHIP_RELS):
            peer = (x ^ dx, y ^ dy, cc)
            sends.append(pltpu.make_async_remote_copy(src_ref=chip_slab, dst_ref=land_slab.at[chip], send_sem=b_send.at[j, 0],
                                                      recv_sem=b_recv.at[j, 0], device_id=peer, device_id_type=MESH))
            sends.append(pltpu.make_async_remote_copy(src_ref=chip_pool, dst_ref=land_pool.at[chip], send_sem=b_send.at[j, 1],
                                                      recv_sem=b_recv.at[j, 1], device_id=peer, device_id_type=MESH))
        for cp in sends:
            cp.start()
        land_slab[chip] = chip_slab[...]
        land_pool[chip] = chip_pool[...]
        for j, (dx, dy) in enumerate(CHIP_RELS):
            peer_chip = 2 * (x ^ dx) + (y ^ dy)
            pltpu.make_async_remote_copy(src_ref=chip_slab, dst_ref=land_slab.at[peer_chip], send_sem=b_send.at[j, 0],
                                         recv_sem=b_recv.at[j, 0], device_id=(x, y, cc), device_id_type=MESH).wait_recv()
            pltpu.make_async_remote_copy(src_ref=chip_pool, dst_ref=land_pool.at[peer_chip], send_sem=b_send.at[j, 1],
                                         recv_sem=b_recv.at[j, 1], device_id=(x, y, cc), device_id_type=MESH).wait_recv()
        red = land_slab[0, 0:RED_ROWS, :]
        ptot = land_pool[0]
        for ch in range(1, N_CHIP):
            red = red + land_slab[ch, 0:RED_ROWS, :]
            ptot = ptot + land_pool[ch]
        red_ref[...] = red
        ptot_ref[...] = ptot
        for ch in range(N_CHIP):
            dmod_ref[2 * dm * ch:2 * dm * (ch + 1), :] = land_slab[ch, RED_ROWS:, :]
        for cp in sends:
            cp.wait_send()

    dma = pltpu.SemaphoreType.DMA
    return _pcall(body, name=name, in_specs=[VMEM_SPEC, VMEM_SPEC], out_specs=[VMEM_SPEC] * 3,
                  out_shape=[_sds((RED_ROWS, w), F32), _sds((N_DEV * dm, w), F32), _sds((prow, pw), F32)],
                  scratch=[pltpu.VMEM((rows, w), F32), pltpu.VMEM((prow, pw), F32), pltpu.VMEM((crow, w), F32),
                           pltpu.VMEM((prow, pw), F32), pltpu.VMEM((N_CHIP, crow, w), F32), pltpu.VMEM((N_CHIP, prow, pw), F32),
                           dma((2,)), dma((2,)), dma((3, 2)), dma((3, 2))], comm=comm)(slab, pool)


def _adamw_math(w, g, m, v):
    mn = ADAM_B1 * m + (1.0 - ADAM_B1) * g
    vn = ADAM_B2 * v + (1.0 - ADAM_B2) * (g * g)
    bc1 = 1.0 / (1.0 - ADAM_B1 ** ADAM_STEP)
    bc2 = 1.0 / (1.0 - ADAM_B2 ** ADAM_STEP)
    return -ADAM_LR * ((mn * bc1) / (jnp.sqrt(vn * bc2) + ADAM_EPS) + ADAM_WD * w), mn, vn


def _small_update(red, dmod_all, pool_total, nb, params, *, name):
    names = list(SMALL)
    dm = dmod_all.shape[0] // N_DEV

    def grad_of(nm, red_ref, dmod_ref, ptot_ref):
        if nm == "b_ada":
            acc = None
            for d in range(N_DEV):
                for s in range(nb):
                    blk = dmod_ref[d * dm + 9 * s:d * dm + 9 * s + 9, :]
                    acc = blk if acc is None else acc + blk
            return jnp.concatenate([acc[k:k + 1, :] for k in range(9)], axis=1)
        if nm == "pool_grp":
            return ptot_ref[...]
        row, lo, n = {"norm_ffn1": (0, 0, D), "norm_mix": (1, 0, D), "norm_ffn2": (2, 0, D), "pool_scale": (3, 0, POOL_W),
                      "q_a_norm": (4, 0, QL), "kv_a_norm": (5, 0, KVL), "q_norm_nope": (6, 0, NOPE),
                      "q_norm_rope": (6, NOPE, ROPE), "k_norm_nope": (7, 0, NOPE), "k_norm_rope": (7, KR_LANE, ROPE)}[nm]
        return red_ref[row:row + 1, lo:lo + n]

    def body(*refs):
        red_ref, dmod_ref, ptot_ref = refs[:3]
        ins = refs[3:3 + 3 * len(names)]
        outs = refs[3 + 3 * len(names):]
        outs[4 * len(names)][...] = red_ref[RED_ROWS - 1:RED_ROWS, LOSS_LANE:]
        for i, nm in enumerate(names):
            g = grad_of(nm, red_ref, dmod_ref, ptot_ref)
            d, mn, vn = _adamw_math(ins[3 * i][...], g, ins[3 * i + 1][...], ins[3 * i + 2][...])
            outs[4 * i][...] = g
            outs[4 * i + 1][...] = d
            outs[4 * i + 2][...] = mn
            outs[4 * i + 3][...] = vn

    flat_in = [a for nm in names for a in params[nm]]
    out_shape = [_sds(params[nm][0].shape, F32) for nm in names for _ in range(4)] + [_sds((1, LANE), F32)]
    res = _pcall(body, name=name, in_specs=[VMEM_SPEC] * (3 + len(flat_in)), out_specs=[VMEM_SPEC] * len(out_shape),
                 out_shape=out_shape)(red, dmod_all, pool_total, *flat_in)
    return {nm: tuple(res[4 * i:4 * i + 4]) for i, nm in enumerate(names)}, res[-1]


def _adamw(w, g, m, v, *, name, comm=None):
    r, c = w.shape
    tr = _pick_rows(r, 256)
    tc = c if tr < r else _pick(c, 256)
    spec = pl.BlockSpec((tr, tc), lambda i, j: (i, j))
    bc1 = 1.0 / (1.0 - ADAM_B1 ** ADAM_STEP)
    bc2 = 1.0 / (1.0 - ADAM_B2 ** ADAM_STEP)

    def body(w_ref, g_ref, m_ref, v_ref, d_ref, mo_ref, vo_ref):
        gv = g_ref[...]
        mn = ADAM_B1 * m_ref[...] + (1.0 - ADAM_B1) * gv
        vn = ADAM_B2 * v_ref[...] + (1.0 - ADAM_B2) * (gv * gv)
        mo_ref[...] = mn
        vo_ref[...] = vn
        d_ref[...] = -ADAM_LR * ((mn * bc1) / (jnp.sqrt(vn * bc2) + ADAM_EPS) + ADAM_WD * w_ref[...])

    out = _sds((r, c), F32)
    return _pcall(body, name=name, out_shape=[out, out, out], grid=(r // tr, c // tc), in_specs=[spec] * 4, out_specs=[spec] * 3,
                  comm=comm)(w, g, m, v)


BIG = ("w_ffn1_in", "w_ffn1_out", "w_in", "w_pool_proj", "w_q_up", "w_kv_up", "w_mla_proj", "w_out", "w_ffn2_in", "w_ffn2_out")
ROW_SHARDED = ("w_ffn1_out", "w_out", "w_ffn2_out")
KERNEL_NAME = {"w_ffn1_in": "ffn1_in", "w_ffn1_out": "ffn1_out", "w_in": "w_in", "w_pool_proj": "pool_proj", "w_q_up": "q_up",
               "w_kv_up": "kv_up", "w_mla_proj": "mla_proj", "w_out": "w_out", "w_ffn2_in": "ffn2_in", "w_ffn2_out": "ffn2_out"}
WEIGHTS = ("w_ada", "b_ada", "norm_ffn1", "w_ffn1_in", "w_ffn1_out", "norm_mix", "w_in", "pool_grp", "pool_scale", "w_pool_proj",
           "q_a_norm", "w_q_up", "kv_a_norm", "w_kv_up", "q_norm_nope", "q_norm_rope", "k_norm_nope", "k_norm_rope", "w_mla_proj",
           "w_out", "norm_ffn2", "w_ffn2_in", "w_ffn2_out")
SMALL = ("b_ada", "norm_ffn1", "norm_mix", "pool_grp", "pool_scale", "q_a_norm", "kv_a_norm", "q_norm_nope", "q_norm_rope",
         "k_norm_nope", "k_norm_rope", "norm_ffn2")


def _assemble(name, stacked):
    if name in ROW_SHARDED:
        return stacked.reshape(stacked.shape[0] * stacked.shape[1], stacked.shape[2])
    return jnp.transpose(stacked, (1, 0, 2)).reshape(stacked.shape[1], stacked.shape[0] * stacked.shape[2])


def _split(name, full):
    if name in ROW_SHARDED:
        return full.reshape(N_CHIP, full.shape[0] // N_CHIP, full.shape[1])
    return jnp.transpose(full.reshape(full.shape[0], N_CHIP, full.shape[1] // N_CHIP), (1, 0, 2))


GU = ("w_ffn1_in", "w_ffn2_in")
NARROW = ("w_in", "w_q_up")


MIX_SMALL = ("w_out", "w_pool_proj", "w_mla_proj", "w_q_up", "w_kv_up")
RIDES = {
    "ada_prologue": (("gather", ("w_ffn1_in",)),),
    "ffn1_up_a": (("gather", ("w_ffn1_out",)),),
    "ffn1_up_b": (("gather", ("w_in",)),),
    "mix_in": (("gather", MIX_SMALL),),
    "attn_fwd": (("gather", ("w_ffn2_in", "w_ffn2_out")),),
    "d_ffn2_h": (("swap", ("w_ffn2_out", "w_ffn2_in")),),
    "attn_bwd": (("exchange", ("w_ffn2_out", "w_ffn2_in")),),
    "d_mix_in": (("swap", MIX_SMALL),),
    "d_mix_h": (("exchange", MIX_SMALL), ("swap", ("w_in",))),
    "ffn1_dact": (("exchange", ("w_in",)), ("swap", ("w_ffn1_out",))),
    "d_ffn1_in_a": (("exchange", ("w_ffn1_out",)),),
    "d_ffn1_in_b": (("swap", ("w_ffn1_in@0",)),),
    "d_ffn1_h": (("exchange", ("w_ffn1_in@0",)), ("swap", ("w_ffn1_in@1",))),
    "bwd_norm1": (("exchange", ("w_ffn1_in@1",)),),
    "small_allreduce": (("join", tuple(n for n in BIG if n != "w_ffn1_in") + ("w_ffn1_in@0", "w_ffn1_in@1")),),
}


def _kname(n):
    base, _, part = n.partition("@")
    return KERNEL_NAME[base] + ("@" + part if part else "")


def _both(comms):
    if len(comms) == 1:
        return comms[0]
    ins, outs, sems, aliases, spans = [], [], [], {}, []
    for c in comms:
        spans.append((len(ins), len(c.ins), len(outs), len(c.out_shapes), len(sems), len(c.sems)))
        aliases.update({len(ins) + i: len(outs) + o for i, o in c.aliases.items()})
        ins, outs, sems = ins + c.ins, outs + c.out_shapes, sems + c.sems

    def each(which):
        def run(i_, o_, s_):
            for c, (ia, ni, oa, no, sa, ns) in zip(comms, spans):
                getattr(c, which)(i_[ia:ia + ni], o_[oa:oa + no], s_[sa:sa + ns])
        return run

    return _Comm(ins, outs, sems, each("start"), each("finish"), aliases)


class _ExchangePlan:
    def __init__(self, shards, cidx):
        self.shards, self.cidx = shards, cidx
        self.W, self.G, self.parts, self.pre, self.reduced, self.joined = {}, {}, {}, {}, {}, {}

    def grad(self, key, g):
        self.G[key] = g

    def rider(self, name):
        comms = []
        for kind, names in RIDES.get(name, ()):
            if kind == "gather":
                comms.append(_gather_comm([self.shards[n] for n in names]))
            elif kind == "swap":
                for n in names:
                    self.parts[n] = self._stacked(n)
                comms.append(_swap_comm([self.parts[n] for n in names]))
            elif kind == "exchange":
                comms.append(_exchange_comm([self.pre[n] for n in names]))
            else:
                comms.append(_join_comm([self.reduced[n] for n in names]))
        return _both(comms) if comms else None

    def landed(self, name, outs):
        at = 0
        for kind, names in RIDES[name]:
            for n, o in zip(names, outs[at:at + len(names)]):
                if kind == "gather":
                    self.W[KERNEL_NAME[n]] = self._to_kernel(n, o)
                elif kind == "swap":
                    self.pre[n] = _add_half(self.parts[n], o, self.cidx, name="rs_add_" + _kname(n))
                elif kind == "exchange":
                    self.reduced[n] = _sum_chips(o, self.cidx, name="rs_sum_" + _kname(n))
                else:
                    self.joined[n] = o
            at += len(names)

    @staticmethod
    def _to_kernel(n, stacked):
        if n in GU:
            return stacked
        full = _assemble(n, stacked)
        return {"w_in": _w_in_to_kernel, "w_q_up": _q_up_to_kernel, "w_kv_up": _kv_up_to_kernel}.get(n, lambda w: w)(full)

    def _stacked(self, n):
        g = self.G[_kname(n)]
        if n.partition("@")[0] in GU:
            return g
        full = {"w_in": _w_in_from_kernel, "w_q_up": _q_up_from_kernel, "w_kv_up": _kv_up_from_kernel}.get(n, lambda w: w)(g)
        return _split(n, full)


def kernel(x, c, positions, w_ada, b_ada, norm_ffn1, w_ffn1_in, w_ffn1_out, norm_mix, w_in, pool_grp, pool_scale, w_pool_proj, q_a_norm, w_q_up, kv_a_norm, w_kv_up, q_norm_nope, q_norm_rope, k_norm_nope, k_norm_rope, w_mla_proj, w_out, norm_ffn2, w_ffn2_in, w_ffn2_out, loss_target, m_w_ada, m_b_ada, m_norm_ffn1, m_w_ffn1_in, m_w_ffn1_out, m_norm_mix, m_w_in, m_pool_grp, m_pool_scale, m_w_pool_proj, m_q_a_norm, m_w_q_up, m_kv_a_norm, m_w_kv_up, m_q_norm_nope, m_q_norm_rope, m_k_norm_nope, m_k_norm_rope, m_w_mla_proj, m_w_out, m_norm_ffn2, m_w_ffn2_in, m_w_ffn2_out, v_w_ada, v_b_ada, v_norm_ffn1, v_w_ffn1_in, v_w_ffn1_out, v_norm_mix, v_w_in, v_pool_grp, v_pool_scale, v_w_pool_proj, v_q_a_norm, v_w_q_up, v_kv_a_norm, v_w_kv_up, v_q_norm_nope, v_q_norm_rope, v_k_norm_nope, v_k_norm_rope, v_w_mla_proj, v_w_out, v_norm_ffn2, v_w_ffn2_in, v_w_ffn2_out):
    args = dict(locals())
    wts = {n: args[n][0] for n in WEIGHTS}
    mom = {n: args["m_" + n][0] for n in WEIGHTS}
    var = {n: args["v_" + n][0] for n in WEIGHTS}
    nb, seq, dm = x.shape
    tokens = nb * seq
    xi, yi, ci = _coords()
    chip = 2 * xi + yi

    cidx = ci.astype(jnp.int32).reshape(1)
    plan = _ExchangePlan({n: wts[n].astype(MX) for n in BIG}, cidx)
    plan.W["pool_grp"] = wts["pool_grp"].astype(MX)

    ncol = w_ada.shape[2]
    b_cols = lax.dynamic_slice_in_dim(wts["b_ada"].reshape(1, -1), chip * ncol, ncol, axis=1)
    c_slots, mod_slots = _run(plan, _ada_prologue, c, wts["w_ada"], b_cols, name="ada_prologue")
    c_all = c_slots[:, :nb].reshape(N_DEV * nb, dm)
    mod3 = jnp.transpose(mod_slots[:, :nb], (1, 0, 2)).reshape(nb, 9, dm)
    P = {"norm_ffn1": wts["norm_ffn1"].reshape(1, dm), "norm_mix": wts["norm_mix"].reshape(1, dm),
         "norm_ffn2": wts["norm_ffn2"].reshape(1, dm), "pool_scale": wts["pool_scale"].reshape(1, POOL_W),
         "q_a_norm": wts["q_a_norm"].reshape(1, QL), "kv_a_norm": wts["kv_a_norm"].reshape(1, KVL),
         "q_gain": _gain_slab(wts["q_norm_nope"].reshape(1, NOPE), wts["q_norm_rope"].reshape(1, ROPE)),
         "k_gain": _gain_slab(wts["k_norm_nope"].reshape(1, NOPE), wts["k_norm_rope"].reshape(1, ROPE))}
    cos, sin = _rope_tables(positions.reshape(tokens))

    loss8, grad_x, stats, d_pool_grp = _layer_fwd_bwd(x.reshape(tokens, dm), loss_target.reshape(tokens, dm), mod3, cos, sin, plan, P)

    slab = _pack_stats(*stats, loss8, name="pack_stats")
    red, dmod_rows, pool_total = _run(plan, _small_allreduce, slab, d_pool_grp.reshape(POOL_G * LANE, LANE), name="small_allreduce")
    grads = {n: plan.joined[n] for n in BIG if n != "w_ffn1_in"}
    grads["w_ffn1_in"] = jnp.concatenate([plan.joined["w_ffn1_in@0"], plan.joined["w_ffn1_in@1"]], axis=0)
    dm_rows = dmod_rows.shape[0] // N_DEV
    dmod_all = dmod_rows.reshape(N_DEV, dm_rows, SLAB_W)[:, :9 * nb].reshape(N_DEV * nb, 9 * dm)
    dmod_cols = lax.dynamic_slice_in_dim(dmod_all, chip * ncol, ncol, axis=1)
    grads["w_ada"] = _ada_bwd(c_all, dmod_cols, name="ada_bwd")

    delta, new_m, new_v = {}, {}, {}
    as2d = lambda a: a.reshape(POOL_G * LANE, LANE) if a.ndim == 4 else a.reshape(1, -1)
    upd, loss_row = _small_update(red, dmod_rows, pool_total, nb,
                                  {n: tuple(as2d(args[p + n]) for p in ("", "m_", "v_")) for n in SMALL}, name="small_update")
    loss = loss_row[0, 0]
    for n in SMALL:
        grads[n], delta[n], new_m[n], new_v[n] = upd[n]
    for n in ("w_ada",) + BIG:
        if n in NARROW:
            res = _adamw(wts[n].T, grads[n].T, mom[n].T, var[n].T, name="adamw_" + n)
            delta[n], new_m[n], new_v[n] = (r.T for r in res)
        else:
            delta[n], new_m[n], new_v[n] = _adamw(wts[n], grads[n], mom[n], var[n], name="adamw_" + n)

    def lead(a, n):
        return a.reshape((1,) + wts[n].shape)

    return (loss, grad_x.reshape(nb, seq, dm), *[lead(grads[n], n) for n in WEIGHTS], *[lead(delta[n], n) for n in WEIGHTS],
            *[lead(new_m[n], n) for n in WEIGHTS], *[lead(new_v[n], n) for n in WEIGHTS])
```

```python
import functools
import math

import jax
import jax.numpy as jnp
from jax import lax
from jax.experimental import pallas as pl
from jax.experimental.pallas import tpu as pltpu

F32 = jnp.float32
MX = jnp.bfloat16

D = 1024
DFF = 2816
NH = 8
POOL_W = 512
POOL_G = 4
QL = 384
KVL = 256
ROPE = 32
NOPE = 64
LANE = 128
SUBLANE = 8
EPS = 1e-6
ATTN_SCALE = 1.0 / math.sqrt(96.0)
NEG = -1e30

Z_UP, Z_QL, Z_KV, Z_KR, Z_GP, Z_GM, Z_W = 0, 512, 896, 1152, 1280, 2304, 3328
KR_LANE = 64
LAT_W = 1280

ADAM_LR, ADAM_B1, ADAM_B2, ADAM_EPS, ADAM_WD, ADAM_STEP = 0.001, 0.9, 0.999, 1e-08, 0.01, 10

VMEM_LIMIT = 48 * 1024 * 1024
MESH = pl.DeviceIdType.MESH
N_DEV = 8
N_CHIP = 4


class _Comm:
    def __init__(self, ins, out_shapes, sems, start, finish, aliases=None):
        self.ins, self.out_shapes, self.sems = list(ins), list(out_shapes), list(sems)
        self.start, self.finish = start, finish
        self.aliases = aliases or {}


def _pcall(body, *, name, out_shape, grid=(), in_specs=None, out_specs=None, scratch=(), grid_spec=None, aliases=None,
           comm=None):
    params = pltpu.CompilerParams(vmem_limit_bytes=VMEM_LIMIT)
    kw = dict(name=name, compiler_params=params)
    if comm is None:
        if aliases:
            kw["input_output_aliases"] = aliases
        if grid_spec is not None:
            return pl.pallas_call(body, grid_spec=grid_spec, out_shape=out_shape, **kw)
        return pl.pallas_call(body, grid=grid, in_specs=in_specs, out_specs=out_specs, scratch_shapes=scratch,
                              out_shape=out_shape, **kw)
    single = not isinstance(out_shape, (list, tuple))
    outs = [out_shape] if single else list(out_shape)
    ospecs = [out_specs] if single else list(out_specs)
    n_in, n_out, n_ci, n_co, n_scr = len(in_specs), len(outs), len(comm.ins), len(comm.out_shapes), len(scratch)
    io = dict(aliases or {})
    io.update({n_in + i: n_out + o for i, o in comm.aliases.items()})

    def riding(*refs):
        ins, cins = refs[:n_in], refs[n_in:n_in + n_ci]
        at = n_in + n_ci
        os_, couts = refs[at:at + n_out], refs[at + n_out:at + n_out + n_co]
        at += n_out + n_co
        scr, csems = refs[at:at + n_scr], refs[at + n_scr:]
        if grid:
            first = functools.reduce(jnp.logical_and, [pl.program_id(d) == 0 for d in range(len(grid))])
            last = functools.reduce(jnp.logical_and, [pl.program_id(d) == grid[d] - 1 for d in range(len(grid))])
            pl.when(first)(lambda: comm.start(cins, couts, csems))
            body(*ins, *os_, *scr)
            pl.when(last)(lambda: comm.finish(cins, couts, csems))
        else:
            comm.start(cins, couts, csems)
            body(*ins, *os_, *scr)
            comm.finish(cins, couts, csems)

    call = pl.pallas_call(riding, grid=grid, in_specs=list(in_specs) + [HBM_SPEC] * n_ci, out_specs=ospecs + [HBM_SPEC] * n_co,
                          out_shape=outs + comm.out_shapes, scratch_shapes=list(scratch) + comm.sems,
                          input_output_aliases=io, **kw)

    def run(*args):
        res = call(*args, *comm.ins)
        main = list(res[:n_out])
        return (main[0] if single else main), list(res[n_out:])

    return run


def _pick(dim, target):
    best = None
    for t in range(LANE, min(dim, target) + 1, LANE):
        if dim % t == 0:
            best = t
    return dim if best is None else best


def _sds(shape, dtype):
    return jax.ShapeDtypeStruct(shape, dtype)


def _dw(a, g, *, name, tm=512, tn=1024, comm=None):
    return _mm(a, g, mode="tn", out_dtype=F32, name=name, tm=tm, tn=tn, tk=a.shape[0], n_outer=True, comm=comm)


def _mm(a, b, *, mode, out_dtype, name, tm=1024, tn=1024, tk=4096, n_outer=False, comm=None):
    if mode == "nn":
        (M, K), (K2, N) = a.shape, b.shape
    elif mode == "nt":
        (M, K), (N, K2) = a.shape, b.shape
    else:
        (K, M), (K2, N) = a.shape, b.shape
    assert K == K2, (name, a.shape, b.shape)
    tm, tn, tk = _pick(M, tm), _pick(N, tn), _pick(K, tk)
    nk = K // tk
    if n_outer:
        ij = lambda g0, g1: (g1, g0)
        grid = (N // tn, M // tm, nk)
    else:
        ij = lambda g0, g1: (g0, g1)
        grid = (M // tm, N // tn, nk)
    if mode == "tn":
        a_spec = pl.BlockSpec((tk, tm), lambda g0, g1, k: (k, ij(g0, g1)[0]))
    else:
        a_spec = pl.BlockSpec((tm, tk), lambda g0, g1, k: (ij(g0, g1)[0], k))
    if mode == "nt":
        b_spec = pl.BlockSpec((tn, tk), lambda g0, g1, k: (ij(g0, g1)[1], k))
    else:
        b_spec = pl.BlockSpec((tk, tn), lambda g0, g1, k: (k, ij(g0, g1)[1]))
    o_spec = pl.BlockSpec((tm, tn), lambda g0, g1, k: ij(g0, g1))
    dn = {"nn": (((1,), (0,)), ((), ())), "nt": (((1,), (1,)), ((), ())), "tn": (((0,), (0,)), ((), ()))}[mode]

    def dot(a_ref, b_ref):
        return lax.dot_general(a_ref[...].astype(MX), b_ref[...].astype(MX), dn, preferred_element_type=F32)

    def body_one(a_ref, b_ref, o_ref):
        o_ref[...] = dot(a_ref, b_ref).astype(o_ref.dtype)

    def body_acc(a_ref, b_ref, o_ref, acc_ref):
        k = pl.program_id(2)
        part = dot(a_ref, b_ref)

        @pl.when(k == 0)
        def _():
            acc_ref[...] = part

        @pl.when(k > 0)
        def _():
            acc_ref[...] += part

        @pl.when(k == nk - 1)
        def _():
            o_ref[...] = acc_ref[...].astype(o_ref.dtype)

    return _pcall(body_one if nk == 1 else body_acc, name=name, out_shape=_sds((M, N), out_dtype), grid=grid,
                  in_specs=[a_spec, b_spec], out_specs=o_spec, scratch=[] if nk == 1 else [pltpu.VMEM((tm, tn), F32)],
                  comm=comm)(a, b)


def _gu_shard(q):
    return (q % 2) * 2 + q // 2


def _ffn_up(h, w_st, *, name, tm=512, comm=None):
    T, dm = h.shape
    hw = w_st.shape[2]
    tm = _pick(T, tm)

    def body(h_ref, wg_ref, wu_ref, gu_ref, a_ref):
        hv = h_ref[...]
        g = jnp.dot(hv, wg_ref[0], preferred_element_type=F32)
        u = jnp.dot(hv, wu_ref[0], preferred_element_type=F32)
        gu_ref[:, :hw] = g.astype(gu_ref.dtype)
        gu_ref[:, hw:] = u.astype(gu_ref.dtype)
        a_ref[...] = (g * _sigmoid(g) * u).astype(a_ref.dtype)

    return _pcall(body, name=name, grid=(T // tm, 2),
                  in_specs=[pl.BlockSpec((tm, dm), lambda i, j: (i, 0)), pl.BlockSpec((1, dm, hw), lambda i, j: (j, 0, 0)),
                            pl.BlockSpec((1, dm, hw), lambda i, j: (2 + j, 0, 0))],
                  out_specs=[pl.BlockSpec((tm, 2 * hw), lambda i, j: (i, j)), pl.BlockSpec((tm, hw), lambda i, j: (i, j))],
                  out_shape=[_sds((T, 4 * hw), MX), _sds((T, 2 * hw), MX)], comm=comm)(h, w_st, w_st)


def _ffn_up_first(x, mod3, gain, w_st, *, sub, name, tm=512, comm=None):
    T, dm = x.shape
    hw = w_st.shape[2]
    tm = _pick(T, tm)
    tps = (T // mod3.shape[0]) // tm

    def body(x_ref, mod_ref, n_ref, wg_ref, wu_ref, h_ref, gu_ref, a_ref):
        xv = x_ref[...]
        xn = xv * _rsq(xv) * n_ref[...]
        hv = (xn * (1.0 + mod_ref[0, 3 * sub + 1:3 * sub + 2, :]) + mod_ref[0, 3 * sub:3 * sub + 1, :]).astype(h_ref.dtype)
        h_ref[...] = hv
        g = jnp.dot(hv, wg_ref[0], preferred_element_type=F32)
        u = jnp.dot(hv, wu_ref[0], preferred_element_type=F32)
        gu_ref[:, :hw] = g.astype(gu_ref.dtype)
        gu_ref[:, hw:] = u.astype(gu_ref.dtype)
        a_ref[...] = (g * _sigmoid(g) * u).astype(a_ref.dtype)

    return _pcall(body, name=name, grid=(T // tm,),
                  in_specs=[_row_spec(tm, dm), pl.BlockSpec((1, 9, dm), lambda i: (i // tps, 0, 0)),
                            pl.BlockSpec((1, dm), lambda i: (0, 0)), pl.BlockSpec((1, dm, hw), lambda i: (0, 0, 0)),
                            pl.BlockSpec((1, dm, hw), lambda i: (2, 0, 0))],
                  out_specs=[_row_spec(tm, dm), pl.BlockSpec((tm, 2 * hw), lambda i: (i, 0)), pl.BlockSpec((tm, hw), lambda i: (i, 0))],
                  out_shape=[_sds((T, dm), MX), _sds((T, 4 * hw), MX), _sds((T, 2 * hw), MX)],
                  comm=comm)(x, mod3, gain, w_st, w_st)


def _ffn_up_second(h, w_st, gu, a, *, name, tm=512, comm=None):
    T, dm = h.shape
    hw = w_st.shape[2]
    tm = _pick(T, tm)

    def body(h_ref, wg_ref, wu_ref, gu_in, a_in, gu_ref, a_ref):
        hv = h_ref[...]
        g = jnp.dot(hv, wg_ref[0], preferred_element_type=F32)
        u = jnp.dot(hv, wu_ref[0], preferred_element_type=F32)
        gu_ref[:, :hw] = g.astype(gu_ref.dtype)
        gu_ref[:, hw:] = u.astype(gu_ref.dtype)
        a_ref[...] = (g * _sigmoid(g) * u).astype(a_ref.dtype)

    return _pcall(body, name=name, grid=(T // tm,),
                  in_specs=[_row_spec(tm, dm), pl.BlockSpec((1, dm, hw), lambda i: (1, 0, 0)),
                            pl.BlockSpec((1, dm, hw), lambda i: (3, 0, 0)), HBM_SPEC, HBM_SPEC],
                  out_specs=[pl.BlockSpec((tm, 2 * hw), lambda i: (i, 1)), pl.BlockSpec((tm, hw), lambda i: (i, 1))],
                  out_shape=[_sds(gu.shape, gu.dtype), _sds(a.shape, a.dtype)], aliases={3: 0, 4: 1},
                  comm=comm)(h, w_st, w_st, gu, a)


def _ffn_dact(df, gu, w_out, *, name, tm=512, comm=None):
    T, dm = df.shape
    hw = gu.shape[1] // 4
    tm = _pick(T, tm)

    def body(df_ref, gu_ref, wo_ref, dgu_ref):
        da = lax.dot_general(df_ref[...], wo_ref[...], (((1,), (1,)), ((), ())), preferred_element_type=F32)
        g = gu_ref[:, :hw].astype(F32)
        u = gu_ref[:, hw:].astype(F32)
        s = _sigmoid(g)
        dgu_ref[:, :hw] = (da * u * (s * (1.0 + g * (1.0 - s)))).astype(dgu_ref.dtype)
        dgu_ref[:, hw:] = (da * (g * s)).astype(dgu_ref.dtype)

    return _pcall(body, name=name, grid=(T // tm, 2),
                  in_specs=[pl.BlockSpec((tm, dm), lambda i, j: (i, 0)), pl.BlockSpec((tm, 2 * hw), lambda i, j: (i, j)),
                            pl.BlockSpec((hw, dm), lambda i, j: (j, 0))],
                  out_specs=pl.BlockSpec((tm, 2 * hw), lambda i, j: (i, j)), out_shape=_sds(gu.shape, MX),
                  comm=comm)(df, gu, w_out)


def _ffn_dh(dgu, w_st, *, name, tm=1024, comm=None):
    T = dgu.shape[0]
    _, dm, hw = w_st.shape
    tm = _pick(T, tm)

    def body(d_ref, w_ref, o_ref, acc_ref):
        q = pl.program_id(1)
        part = lax.dot_general(d_ref[...], w_ref[0], (((1,), (1,)), ((), ())), preferred_element_type=F32)

        @pl.when(q == 0)
        def _():
            acc_ref[...] = part

        @pl.when(jnp.logical_and(q > 0, q < 3))
        def _():
            acc_ref[...] += part

        @pl.when(q == 3)
        def _():
            o_ref[...] = acc_ref[...] + part

    return _pcall(body, name=name, grid=(T // tm, 4),
                  in_specs=[pl.BlockSpec((tm, hw), lambda i, q: (i, q)), pl.BlockSpec((1, dm, hw), lambda i, q: (_gu_shard(q), 0, 0))],
                  out_specs=pl.BlockSpec((tm, dm), lambda i, q: (i, 0)), out_shape=_sds((T, dm), F32),
                  scratch=[pltpu.VMEM((tm, dm), F32)], comm=comm)(dgu, w_st)


def _ffn_dw_in(h, dgu, *, name, rows=None, tm=512, comm=None):
    T, dm = h.shape
    hw = dgu.shape[1] // 4
    first, count = rows if rows is not None else (0, dm)
    tm = _pick(count, tm)
    skip = first // tm

    def body(h_ref, d_ref, o_ref):
        o_ref[0] = lax.dot_general(h_ref[...], d_ref[...], (((0,), (0,)), ((), ())), preferred_element_type=F32)

    return _pcall(body, name=name, grid=(4, count // tm),
                  in_specs=[pl.BlockSpec((T, tm), lambda q, i: (0, skip + i)), pl.BlockSpec((T, hw), lambda q, i: (0, q))],
                  out_specs=pl.BlockSpec((1, tm, hw), lambda q, i: (_gu_shard(q), i, 0)),
                  out_shape=_sds((4, count, hw), F32), comm=comm)(h, dgu)


def _rsq(x):
    return lax.rsqrt(jnp.mean(x * x, axis=-1, keepdims=True) + EPS)


def _sigmoid(x):
    return 1.0 / (1.0 + jnp.exp(-x))


def _row_spec(tm, w):
    return pl.BlockSpec((tm, w), lambda i: (i, 0))


def _mm_resid_norm(a, w, x_prev, mod3, gain, *, sub, coef, name, tm=512, comm=None):
    T, k = a.shape
    dm = w.shape[1]
    tm = _pick(T, tm)
    tps = (T // mod3.shape[0]) // tm

    def body(a_ref, w_ref, x_ref, mod_ref, n_ref, f_ref, xo_ref, h_ref):
        f = jnp.dot(a_ref[...], w_ref[...], preferred_element_type=F32)
        f_ref[...] = f
        x = x_ref[...] + coef * mod_ref[0, 3 * sub - 1:3 * sub, :] * f
        xo_ref[...] = x
        xn = x * _rsq(x) * n_ref[...]
        h_ref[...] = (xn * (1.0 + mod_ref[0, 3 * sub + 1:3 * sub + 2, :]) + mod_ref[0, 3 * sub:3 * sub + 1, :]).astype(h_ref.dtype)

    row = _row_spec(tm, dm)
    return _pcall(body, name=name, grid=(T // tm,),
                  in_specs=[_row_spec(tm, k), pl.BlockSpec((k, dm), lambda i: (0, 0)), row,
                            pl.BlockSpec((1, 9, dm), lambda i: (i // tps, 0, 0)), pl.BlockSpec((1, dm), lambda i: (0, 0))],
                  out_specs=[row, row, row], out_shape=[_sds((T, dm), F32), _sds((T, dm), F32), _sds((T, dm), MX)],
                  comm=comm)(a, w, x_prev, mod3, gain)


def _mm_loss(a, w, x2, tgt, mod3, *, name, tm=512):
    T, k = a.shape
    dm = w.shape[1]
    tm = _pick(T, tm)
    tps = (T // mod3.shape[0]) // tm
    mod_spec = pl.BlockSpec((1, 9, dm), lambda i: (i // tps, 0, 0))
    row = _row_spec(tm, dm)
    stat_spec = pl.BlockSpec((1, SUBLANE, dm), lambda i: (i // tps, 0, 0))
    loss_spec = pl.BlockSpec((SUBLANE, LANE), lambda i: (0, 0))

    def body(a_ref, w_ref, x_ref, t_ref, mod_ref, dy_ref, df_ref, st_ref, loss_ref):
        i = pl.program_id(0)
        g = mod_ref[0, 8:9, :]
        f = jnp.dot(a_ref[...], w_ref[...], preferred_element_type=F32)
        err = x_ref[...] + 0.5 * g * f - t_ref[...]
        dy = err * (1.0 / dm)
        dy_ref[...] = dy
        df_ref[...] = (0.5 * g * dy).astype(df_ref.dtype)
        dgate = jnp.sum(0.5 * dy * f, axis=0, keepdims=True)
        part = 0.5 * jnp.sum(jnp.sum(err * err, axis=0, keepdims=True), axis=1, keepdims=True) * (1.0 / dm)

        @pl.when(i % tps == 0)
        def _():
            st_ref[...] = jnp.zeros_like(st_ref)

        @pl.when(i == 0)
        def _():
            loss_ref[...] = jnp.zeros_like(loss_ref)

        st_ref[0, 0:1, :] += dgate
        loss_ref[...] += jnp.broadcast_to(part, loss_ref.shape)

    return _pcall(body, name=name, grid=(T // tm,),
                  in_specs=[_row_spec(tm, k), pl.BlockSpec((k, dm), lambda i: (0, 0)), row, row, mod_spec],
                  out_specs=[row, row, stat_spec, loss_spec],
                  out_shape=[_sds((T, dm), F32), _sds((T, dm), MX), _sds((mod3.shape[0], SUBLANE, dm), F32),
                             _sds((SUBLANE, LANE), F32)])(a, w, x2, tgt, mod3)


def _norm_bwd_tail(dhv, x_ref, dxi_ref, f_ref, mod_ref, n_ref, dx_ref, df_ref, st_ref, *, first, sub, coef):
    x = x_ref[...]
    r = _rsq(x)
    xhat = x * r
    n = n_ref[...]
    d_shift = jnp.sum(dhv, axis=0, keepdims=True)
    d_scale = jnp.sum(dhv * (xhat * n), axis=0, keepdims=True)
    dxn = dhv * (1.0 + mod_ref[0, 3 * sub + 1:3 * sub + 2, :])
    d_gain = jnp.sum(dxn * xhat, axis=0, keepdims=True)
    dxhat = dxn * n
    dx = dxi_ref[...] + r * (dxhat - xhat * jnp.mean(dxhat * xhat, axis=-1, keepdims=True))
    dx_ref[...] = dx

    @pl.when(first)
    def _():
        st_ref[...] = jnp.zeros_like(st_ref)

    st_ref[0, 0:1, :] += d_shift
    st_ref[0, 1:2, :] += d_scale
    st_ref[0, 2:3, :] += d_gain
    if f_ref is not None:
        st_ref[0, 3:4, :] += jnp.sum(coef * dx * f_ref[...], axis=0, keepdims=True)
        df_ref[...] = (coef * mod_ref[0, 3 * sub - 1:3 * sub, :] * dx).astype(df_ref.dtype)


def _ffn_dh_norm(dgu, w_st, x_cur, dx_in, f_prev, mod3, gain, *, sub, coef, name, tm=512, comm=None):
    T = dgu.shape[0]
    _, dm, hw = w_st.shape
    tm = _pick(T, tm)
    nb = mod3.shape[0]
    tps = (T // nb) // tm

    def body(d_ref, w_ref, x_ref, dxi_ref, f_ref, mod_ref, n_ref, dx_ref, df_ref, st_ref, acc_ref):
        i = pl.program_id(0)
        q = pl.program_id(1)
        part = lax.dot_general(d_ref[...], w_ref[0], (((1,), (1,)), ((), ())), preferred_element_type=F32)

        @pl.when(q == 0)
        def _():
            acc_ref[...] = part

        @pl.when(q > 0)
        def _():
            acc_ref[...] += part

        @pl.when(q == 3)
        def _():
            _norm_bwd_tail(acc_ref[...], x_ref, dxi_ref, f_ref, mod_ref, n_ref, dx_ref, df_ref, st_ref,
                           first=i % tps == 0, sub=sub, coef=coef)

    row = pl.BlockSpec((tm, dm), lambda i, q: (i, 0))
    return _pcall(body, name=name, grid=(T // tm, 4),
                  in_specs=[pl.BlockSpec((tm, hw), lambda i, q: (i, q)), pl.BlockSpec((1, dm, hw), lambda i, q: (_gu_shard(q), 0, 0)),
                            row, row, row, pl.BlockSpec((1, 9, dm), lambda i, q: (i // tps, 0, 0)),
                            pl.BlockSpec((1, dm), lambda i, q: (0, 0))],
                  out_specs=[row, row, pl.BlockSpec((1, SUBLANE, dm), lambda i, q: (i // tps, 0, 0))],
                  out_shape=[_sds((T, dm), F32), _sds((T, dm), MX), _sds((nb, SUBLANE, dm), F32)],
                  scratch=[pltpu.VMEM((tm, dm), F32)], comm=comm)(dgu, w_st, x_cur, dx_in, f_prev, mod3, gain)


def _mm_nt_norm(a, b, x_cur, dx_in, f_prev, mod3, gain, *, sub, coef, name, tm=512, comm=None):
    T, k = a.shape
    dm = b.shape[0]
    tm = _pick(T, tm)
    nb = mod3.shape[0]
    tps = (T // nb) // tm

    def body(a_ref, b_ref, x_ref, dxi_ref, f_ref, mod_ref, n_ref, dx_ref, df_ref, st_ref):
        dh = lax.dot_general(a_ref[...], b_ref[...], (((1,), (1,)), ((), ())), preferred_element_type=F32)
        _norm_bwd_tail(dh, x_ref, dxi_ref, f_ref, mod_ref, n_ref, dx_ref, df_ref, st_ref,
                       first=pl.program_id(0) % tps == 0, sub=sub, coef=coef)

    row = _row_spec(tm, dm)
    return _pcall(body, name=name, grid=(T // tm,),
                  in_specs=[_row_spec(tm, k), pl.BlockSpec((dm, k), lambda i: (0, 0)), row, row, row,
                            pl.BlockSpec((1, 9, dm), lambda i: (i // tps, 0, 0)), pl.BlockSpec((1, dm), lambda i: (0, 0))],
                  out_specs=[row, row, pl.BlockSpec((1, SUBLANE, dm), lambda i: (i // tps, 0, 0))],
                  out_shape=[_sds((T, dm), F32), _sds((T, dm), MX), _sds((nb, SUBLANE, dm), F32)],
                  comm=comm)(a, b, x_cur, dx_in, f_prev, mod3, gain)


def _bwd_block(x_cur, dh, dx_in, f_prev, mod3, gain, *, sub, coef, name, tm=256, comm=None):
    T, dm = x_cur.shape
    nb = mod3.shape[0]
    tps = (T // nb) // tm
    has_f = f_prev is not None
    mod_spec = pl.BlockSpec((1, 9, dm), lambda i: (i // tps, 0, 0))
    vec_spec = pl.BlockSpec((1, dm), lambda i: (0, 0))
    stat_spec = pl.BlockSpec((1, SUBLANE, dm), lambda i: (i // tps, 0, 0))
    row = _row_spec(tm, dm)

    def body(*refs):
        if has_f:
            x_ref, dh_ref, dxi_ref, f_ref, mod_ref, n_ref, dx_ref, df_ref, st_ref = refs
        else:
            x_ref, dh_ref, dxi_ref, mod_ref, n_ref, dx_ref, st_ref = refs
            f_ref = df_ref = None
        _norm_bwd_tail(dh_ref[...], x_ref, dxi_ref, f_ref, mod_ref, n_ref, dx_ref, df_ref, st_ref,
                       first=pl.program_id(0) % tps == 0, sub=sub, coef=coef)

    st_shape = _sds((nb, SUBLANE, dm), F32)
    if has_f:
        return _pcall(body, name=name, grid=(T // tm,), in_specs=[row, row, row, row, mod_spec, vec_spec],
                      out_specs=[row, row, stat_spec],
                      out_shape=[_sds((T, dm), F32), _sds((T, dm), MX), st_shape], comm=comm)(x_cur, dh, dx_in, f_prev, mod3, gain)
    return _pcall(body, name=name, grid=(T // tm,), in_specs=[row, row, row, mod_spec, vec_spec],
                  out_specs=[row, stat_spec], out_shape=[_sds((T, dm), F32), st_shape], comm=comm)(x_cur, dh, dx_in, mod3, gain)


def _mix_out_fwd(z, br_pool, attn, w_mla, w_out, x_prev, mod3, gain, *, sub, name, tm=512):
    T = z.shape[0]
    dm = w_out.shape[1]
    tm = _pick(T, tm)
    tps = (T // mod3.shape[0]) // tm
    whole = lambda a: pl.BlockSpec(a.shape, lambda i: (0, 0))

    def body(z_ref, bp_ref, at_ref, wm_ref, wo_ref, x_ref, mod_ref, n_ref, bm_ref, mg_ref, mo_ref, xo_ref, h_ref):
        bm = jnp.dot(at_ref[...], wm_ref[...], preferred_element_type=F32).astype(MX)
        bm_ref[...] = bm
        gp = z_ref[:, Z_GP:Z_GP + D].astype(F32)
        gm = z_ref[:, Z_GM:Z_GM + D].astype(F32)
        merged = (_sigmoid(gp) * bp_ref[...] + _sigmoid(gm) * bm).astype(MX)
        mg_ref[...] = merged
        mo = jnp.dot(merged, wo_ref[...], preferred_element_type=F32)
        mo_ref[...] = mo
        x = x_ref[...] + mod_ref[0, 3 * sub - 1:3 * sub, :] * mo
        xo_ref[...] = x
        xn = x * _rsq(x) * n_ref[...]
        h_ref[...] = (xn * (1.0 + mod_ref[0, 3 * sub + 1:3 * sub + 2, :]) + mod_ref[0, 3 * sub:3 * sub + 1, :]).astype(h_ref.dtype)

    row = _row_spec(tm, dm)
    return _pcall(body, name=name, grid=(T // tm,),
                  in_specs=[_row_spec(tm, Z_W), row, _row_spec(tm, attn.shape[1]), whole(w_mla), whole(w_out), row,
                            pl.BlockSpec((1, 9, dm), lambda i: (i // tps, 0, 0)), pl.BlockSpec((1, dm), lambda i: (0, 0))],
                  out_specs=[row, row, row, row, row],
                  out_shape=[_sds((T, dm), MX), _sds((T, dm), MX), _sds((T, dm), F32), _sds((T, dm), F32), _sds((T, dm), MX)],
                  )(z, br_pool, attn, w_mla, w_out, x_prev, mod3, gain)


def _mix_out_bwd(dmo, w_out, z, br_pool, br_mla, w_mla, *, name, tm=512):
    T = z.shape[0]
    tm = _pick(T, tm)
    whole = lambda a: pl.BlockSpec(a.shape, lambda i: (0, 0))
    nt = (((1,), (1,)), ((), ()))

    def body(dmo_ref, wo_ref, z_ref, bp_ref, bm_ref, wm_ref, dbp_ref, dbm_ref, dg_ref, dat_ref):
        dm = lax.dot_general(dmo_ref[...], wo_ref[...], nt, preferred_element_type=F32)
        sp = _sigmoid(z_ref[:, Z_GP:Z_GP + D].astype(F32))
        sm = _sigmoid(z_ref[:, Z_GM:Z_GM + D].astype(F32))
        dbp_ref[...] = (dm * sp).astype(dbp_ref.dtype)
        dbm = (dm * sm).astype(MX)
        dbm_ref[...] = dbm
        dg_ref[:, :D] = (dm * bp_ref[...].astype(F32) * sp * (1.0 - sp)).astype(dg_ref.dtype)
        dg_ref[:, D:] = (dm * bm_ref[...].astype(F32) * sm * (1.0 - sm)).astype(dg_ref.dtype)
        dat_ref[...] = lax.dot_general(dbm, wm_ref[...], nt, preferred_element_type=F32)

    row = _row_spec(tm, D)
    return _pcall(body, name=name, grid=(T // tm,),
                  in_specs=[row, whole(w_out), _row_spec(tm, Z_W), row, row, whole(w_mla)],
                  out_specs=[row, row, _row_spec(tm, 2 * D), _row_spec(tm, w_mla.shape[0])],
                  out_shape=[_sds((T, D), MX), _sds((T, D), MX), _sds((T, 2 * D), MX), _sds((T, w_mla.shape[0]), F32)],
                  )(dmo, w_out, z, br_pool, br_mla, w_mla)


def _shift_down(x, k, row):
    return jnp.where(row >= k, pltpu.roll(x, k, 0), 0.0)


def _shift_up(x, k, row, n):
    return jnp.where(row < n - k, pltpu.roll(x, n - k, 0), 0.0)


def _pool_fwd(z, pool_grp, pool_scale, *, nb, name):
    T = z.shape[0]
    S = T // nb
    blk = pl.BlockSpec((S, LANE), lambda b, g: (b, g))

    def body(u_ref, w_ref, s_ref, pooled_ref, mixed_ref, scaled_ref):
        g = pl.program_id(1)
        u = u_ref[...].astype(F32)
        row = lax.broadcasted_iota(jnp.int32, u.shape, 0)
        s2 = u + _shift_down(u, 1, row)
        s4 = s2 + _shift_down(s2, 2, row)
        s8 = s4 + _shift_down(s4, 4, row)
        s16 = s8 + _shift_down(s8, 8, row)
        win = jnp.where(g == 0, s2, jnp.where(g == 1, s4, jnp.where(g == 2, s8, s16)))
        width = lax.shift_left(jnp.int32(2), g)
        cnt = jnp.minimum(row + 1, width).astype(F32)
        pooled = (win / cnt - u).astype(MX)
        pooled_ref[...] = pooled
        mixed = jnp.dot(pooled, w_ref[0], preferred_element_type=F32)
        mixed_ref[...] = mixed
        scaled_ref[...] = (mixed * s_ref[...]).astype(scaled_ref.dtype)

    return _pcall(body, name=name, grid=(nb, POOL_G),
                  in_specs=[blk, pl.BlockSpec((1, LANE, LANE), lambda b, g: (g, 0, 0)),
                            pl.BlockSpec((1, LANE), lambda b, g: (0, g))],
                  out_specs=[blk, blk, blk],
                  out_shape=[_sds((T, POOL_W), MX), _sds((T, POOL_W), F32), _sds((T, POOL_W), MX)])(z, pool_grp, pool_scale)


def _pool_bwd(dscaled, mixed, pooled, pool_grp, pool_scale, *, nb, name):
    T = dscaled.shape[0]
    S = T // nb
    blk = pl.BlockSpec((S, LANE), lambda g, b: (b, g))

    def body(ds_ref, mixed_ref, pooled_ref, w_ref, s_ref, du_ref, dw_ref, dsc_ref):
        g = pl.program_id(0)
        b = pl.program_id(1)
        ds = ds_ref[...]
        dsc_ref[0] = jnp.sum(ds * mixed_ref[...], axis=0, keepdims=True)
        dmixed = (ds * s_ref[...]).astype(MX)
        dw = lax.dot_general(pooled_ref[...], dmixed, (((0,), (0,)), ((), ())), preferred_element_type=F32)

        @pl.when(b == 0)
        def _():
            dw_ref[0] = dw

        @pl.when(b > 0)
        def _():
            dw_ref[0] += dw
        dpooled = lax.dot_general(dmixed, w_ref[0], (((1,), (1,)), ((), ())), preferred_element_type=F32)
        row = lax.broadcasted_iota(jnp.int32, dpooled.shape, 0)
        width = lax.shift_left(jnp.int32(2), g)
        q = dpooled / jnp.minimum(row + 1, width).astype(F32)
        r2 = q + _shift_up(q, 1, row, S)
        r4 = r2 + _shift_up(r2, 2, row, S)
        r8 = r4 + _shift_up(r4, 4, row, S)
        r16 = r8 + _shift_up(r8, 8, row, S)
        win = jnp.where(g == 0, r2, jnp.where(g == 1, r4, jnp.where(g == 2, r8, r16)))
        du_ref[...] = (win - dpooled).astype(du_ref.dtype)

    return _pcall(body, name=name, grid=(POOL_G, nb),
                  in_specs=[blk, blk, blk, pl.BlockSpec((1, LANE, LANE), lambda g, b: (g, 0, 0)),
                            pl.BlockSpec((1, LANE), lambda g, b: (0, g))],
                  out_specs=[blk, pl.BlockSpec((1, LANE, LANE), lambda g, b: (g, 0, 0)),
                             pl.BlockSpec((1, 1, LANE), lambda g, b: (b, 0, g))],
                  out_shape=[_sds((T, POOL_W), MX), _sds((POOL_G, LANE, LANE), F32), _sds((nb, 1, POOL_W), F32)],
                  )(dscaled, mixed, pooled, pool_grp, pool_scale)


def _lane_masks(shape):
    lane = lax.broadcasted_iota(jnp.int32, shape, len(shape) - 1)
    m_n = lane < NOPE
    m_r = jnp.logical_and(lane >= KR_LANE, lane < KR_LANE + ROPE)
    first_half = lane < KR_LANE + ROPE // 2
    return m_n, m_r, first_half


def _rot(y, first_half):
    return jnp.where(first_half, -pltpu.roll(y, LANE - ROPE // 2, 1), pltpu.roll(y, ROPE // 2, 1))


def _rot_t(v, first_half, m_r):
    return jnp.where(m_r, jnp.where(first_half, pltpu.roll(v, LANE - ROPE // 2, 1), -pltpu.roll(v, ROPE // 2, 1)), 0.0)


def _mla_in_fwd(z, w_q, w_kv, qa_gain, kva_gain, cos, sin, q_gain, k_gain, *, name, tm=256):
    T = z.shape[0]
    slab = pl.BlockSpec((tm, LANE), lambda i: (i, 0))
    vec = pl.BlockSpec((1, LANE), lambda i: (0, 0))
    whole = lambda a: pl.BlockSpec(a.shape, lambda i: (0, 0))

    def body(z_ref, wq_ref, wkv_ref, qa_ref, kva_ref, cos_ref, sin_ref, qg_ref, kg_ref, qn_ref, kvn_ref, q_ref, k_ref, v_ref):
        qn, kvn, qp, kvp = _latents_up(z_ref, wq_ref, wkv_ref, qa_ref, kva_ref)
        qn_ref[...] = qn
        kvn_ref[...] = kvn
        m_n, m_r, first_half = _lane_masks((tm, LANE))
        c = cos_ref[...]
        s = sin_ref[...]
        qg = qg_ref[...]
        kg = kg_ref[...]
        xr = z_ref[:, Z_KR:Z_KR + LANE].astype(F32)
        rr = lax.rsqrt(jnp.sum(xr * xr, axis=-1, keepdims=True) * (1.0 / ROPE) + EPS)
        yr = xr * rr * kg
        kr = jnp.where(m_r, yr * c + _rot(yr, first_half) * s, 0.0)
        for h in range(NH):
            x = qp[:, h * LANE:(h + 1) * LANE]
            x2 = x * x
            rn = lax.rsqrt(jnp.sum(jnp.where(m_n, x2, 0.0), axis=-1, keepdims=True) * (1.0 / NOPE) + EPS)
            rq = lax.rsqrt(jnp.sum(jnp.where(m_r, x2, 0.0), axis=-1, keepdims=True) * (1.0 / ROPE) + EPS)
            y = x * jnp.where(m_n, rn, jnp.where(m_r, rq, 0.0)) * qg
            q_ref[:, h * LANE:(h + 1) * LANE] = (y * c + _rot(y, first_half) * s).astype(q_ref.dtype)
            xk = kvp[:, h * LANE:(h + 1) * LANE]
            rk = lax.rsqrt(jnp.sum(jnp.where(m_n, xk * xk, 0.0), axis=-1, keepdims=True) * (1.0 / NOPE) + EPS)
            k_ref[:, h * LANE:(h + 1) * LANE] = (jnp.where(m_n, xk * rk * kg, 0.0) + kr).astype(k_ref.dtype)
        v_ref[...] = kvp[:, NH * LANE:].astype(v_ref.dtype)

    return _pcall(body, name=name, grid=(T // tm,),
                  in_specs=[_row_spec(tm, LAT_W), whole(w_q), whole(w_kv), whole(qa_gain), whole(kva_gain), slab, slab, vec, vec],
                  out_specs=[_row_spec(tm, QL), _row_spec(tm, KVL), _row_spec(tm, NH * LANE), _row_spec(tm, NH * LANE),
                             _row_spec(tm, NH * NOPE)],
                  out_shape=[_sds((T, QL), MX), _sds((T, KVL), MX), _sds((T, NH * LANE), MX), _sds((T, NH * LANE), MX),
                             _sds((T, NH * NOPE), MX)],
                  )(z, w_q, w_kv, qa_gain, kva_gain, cos, sin, q_gain, k_gain)


def _latents_up(z_ref, wq_ref, wkv_ref, qa_ref, kva_ref):
    ql = z_ref[:, Z_QL:Z_QL + QL].astype(F32)
    kvl = z_ref[:, Z_KV:Z_KV + KVL].astype(F32)
    qn = (ql * _rsq(ql) * qa_ref[...]).astype(MX)
    kvn = (kvl * _rsq(kvl) * kva_ref[...]).astype(MX)
    return qn, kvn, jnp.dot(qn, wq_ref[...], preferred_element_type=F32), jnp.dot(kvn, wkv_ref[...], preferred_element_type=F32)


def _mla_in_bwd(dq, dk, dv, z, w_q, w_kv, qa_gain, kva_gain, cos, sin, q_gain, k_gain, *, nb, name, tm=256):
    T = dq.shape[0]
    tps = (T // nb) // tm
    slab = pl.BlockSpec((tm, LANE), lambda i: (i, 0))
    vec = pl.BlockSpec((1, LANE), lambda i: (0, 0))
    whole = lambda a: pl.BlockSpec(a.shape, lambda i: (0, 0))

    def latent_bwd(x, dy, gain):
        r = _rsq(x)
        xhat = x * r
        dxhat = dy * gain
        return r * (dxhat - xhat * jnp.mean(dxhat * xhat, axis=-1, keepdims=True)), jnp.sum(dy * xhat, axis=0, keepdims=True)

    def body(dq_ref, dk_ref, dv_ref, z_ref, wq_ref, wkv_ref, qa_ref, kva_ref, cos_ref, sin_ref, qg_ref, kg_ref,
             dqp_ref, dkvp_ref, dkr_ref, dql_ref, dkvl_ref, st_ref, sq_ref, sk_ref):
        i = pl.program_id(0)
        _, _, qp, kvp = _latents_up(z_ref, wq_ref, wkv_ref, qa_ref, kva_ref)
        m_n, m_r, first_half = _lane_masks((tm, LANE))
        c = cos_ref[...]
        s = sin_ref[...]
        qg = qg_ref[...]
        kg = kg_ref[...]
        dqg = jnp.zeros((1, LANE), F32)
        dkg = jnp.zeros((1, LANE), F32)
        dkr_sum = jnp.zeros((tm, LANE), F32)
        for h in range(NH):
            x = qp[:, h * LANE:(h + 1) * LANE]
            x2 = x * x
            rn = lax.rsqrt(jnp.sum(jnp.where(m_n, x2, 0.0), axis=-1, keepdims=True) * (1.0 / NOPE) + EPS)
            rq = lax.rsqrt(jnp.sum(jnp.where(m_r, x2, 0.0), axis=-1, keepdims=True) * (1.0 / ROPE) + EPS)
            rfac = jnp.where(m_n, rn, jnp.where(m_r, rq, 0.0))
            xhat = x * rfac
            do = dq_ref[:, h * LANE:(h + 1) * LANE]
            dy = do * c + _rot_t(do * s, first_half, m_r)
            dqg = dqg + jnp.sum(dy * xhat, axis=0, keepdims=True)
            dxhat = dy * qg
            t = dxhat * xhat
            mean_n = jnp.sum(jnp.where(m_n, t, 0.0), axis=-1, keepdims=True) * (1.0 / NOPE)
            mean_r = jnp.sum(jnp.where(m_r, t, 0.0), axis=-1, keepdims=True) * (1.0 / ROPE)
            dqp_ref[:, h * LANE:(h + 1) * LANE] = (
                rfac * (dxhat - xhat * jnp.where(m_n, mean_n, jnp.where(m_r, mean_r, 0.0)))).astype(dqp_ref.dtype)

            xk = kvp[:, h * LANE:(h + 1) * LANE]
            rk = lax.rsqrt(jnp.sum(jnp.where(m_n, xk * xk, 0.0), axis=-1, keepdims=True) * (1.0 / NOPE) + EPS)
            khat = jnp.where(m_n, xk * rk, 0.0)
            dko = dk_ref[:, h * LANE:(h + 1) * LANE]
            dkn = jnp.where(m_n, dko, 0.0)
            dkg = dkg + jnp.sum(dkn * khat, axis=0, keepdims=True)
            dkhat = dkn * kg
            mean_k = jnp.sum(dkhat * khat, axis=-1, keepdims=True) * (1.0 / NOPE)
            dkvp_ref[:, h * LANE:(h + 1) * LANE] = jnp.where(m_n, rk * (dkhat - khat * mean_k), 0.0).astype(dkvp_ref.dtype)
            dkr_sum = dkr_sum + jnp.where(m_r, dko, 0.0)
        dkvp_ref[:, NH * LANE:] = dv_ref[...].astype(dkvp_ref.dtype)

        xr = z_ref[:, Z_KR:Z_KR + LANE].astype(F32)
        rr = lax.rsqrt(jnp.sum(xr * xr, axis=-1, keepdims=True) * (1.0 / ROPE) + EPS)
        rhat = xr * rr
        dyr = dkr_sum * c + _rot_t(dkr_sum * s, first_half, m_r)
        dkg = dkg + jnp.sum(dyr * rhat, axis=0, keepdims=True)
        drhat = dyr * kg
        mean_kr = jnp.sum(drhat * rhat, axis=-1, keepdims=True) * (1.0 / ROPE)
        dkr_ref[...] = jnp.where(m_r, rr * (drhat - rhat * mean_kr), 0.0).astype(dkr_ref.dtype)

        nt = (((1,), (1,)), ((), ()))
        dqn = lax.dot_general(dqp_ref[...], wq_ref[...], nt, preferred_element_type=F32)
        dkvn = lax.dot_general(dkvp_ref[...], wkv_ref[...], nt, preferred_element_type=F32)
        dql, dqa = latent_bwd(z_ref[:, Z_QL:Z_QL + QL].astype(F32), dqn, qa_ref[...])
        dkvl, dkva = latent_bwd(z_ref[:, Z_KV:Z_KV + KVL].astype(F32), dkvn, kva_ref[...])
        dql_ref[...] = dql.astype(dql_ref.dtype)
        dkvl_ref[...] = dkvl.astype(dkvl_ref.dtype)

        @pl.when(i % tps == 0)
        def _():
            st_ref[...] = jnp.zeros_like(st_ref)
            sq_ref[...] = jnp.zeros_like(sq_ref)
            sk_ref[...] = jnp.zeros_like(sk_ref)

        st_ref[0, 0:1, :] += dqg
        st_ref[0, 1:2, :] += dkg
        sq_ref[0, 0:1, :] += dqa
        sk_ref[0, 0:1, :] += dkva

    stat = lambda w: pl.BlockSpec((1, SUBLANE, w), lambda i: (i // tps, 0, 0))
    return _pcall(body, name=name, grid=(T // tm,),
                  in_specs=[_row_spec(tm, NH * LANE), _row_spec(tm, NH * LANE), _row_spec(tm, NH * NOPE), _row_spec(tm, LAT_W),
                            whole(w_q), whole(w_kv), whole(qa_gain), whole(kva_gain), slab, slab, vec, vec],
                  out_specs=[_row_spec(tm, NH * LANE), _row_spec(tm, NH * LANE + NH * NOPE), slab, _row_spec(tm, QL),
                             _row_spec(tm, KVL), stat(LANE), stat(QL), stat(KVL)],
                  out_shape=[_sds((T, NH * LANE), MX), _sds((T, NH * LANE + NH * NOPE), MX), _sds((T, LANE), MX),
                             _sds((T, QL), MX), _sds((T, KVL), MX), _sds((nb, SUBLANE, LANE), F32),
                             _sds((nb, SUBLANE, QL), F32), _sds((nb, SUBLANE, KVL), F32)],
                  )(dq, dk, dv, z, w_q, w_kv, qa_gain, kva_gain, cos, sin, q_gain, k_gain)


def _lower_triangle(t):
    return lax.broadcasted_iota(jnp.int32, (t, t), 1) <= lax.broadcasted_iota(jnp.int32, (t, t), 0)


def _attn_fwd(q, k, v, *, nb, name, tq=512, comm=None):
    T = q.shape[0]
    S = T // nb
    tq = _pick(S, tq)
    nq = S // tq
    tk = tq
    npair = NH // 2

    def body(q_ref, k_ref, v_ref, o_ref, lse_ref):
        qi = pl.program_id(2)
        lane = lax.broadcasted_iota(jnp.int32, (tq, LANE), 1)
        qs = [q_ref[:, hh * LANE:(hh + 1) * LANE] for hh in range(2)]

        def block(j, carry, diagonal):
            k0 = pl.multiple_of(j * tk, tk)
            vb = v_ref[pl.ds(k0, tk), :]
            new = []
            for hh in range(2):
                m, l, acc = carry[hh]
                kb = k_ref[pl.ds(k0, tk), hh * LANE:(hh + 1) * LANE]
                s = lax.dot_general(qs[hh], kb, (((1,), (1,)), ((), ())), preferred_element_type=F32) * ATTN_SCALE
                if diagonal:
                    s = jnp.where(_lower_triangle(tq), s, NEG)
                m_new = jnp.maximum(m, jnp.max(s, axis=-1, keepdims=True))
                p = jnp.exp(s - m_new)
                alpha = jnp.exp(m - m_new)
                l = alpha * l + jnp.sum(p, axis=-1, keepdims=True)
                acc = alpha * acc + jnp.dot(p.astype(MX), vb, preferred_element_type=F32)
                new.append((m_new, l, acc))
            return tuple(new)

        init = tuple((jnp.full((tq, 1), NEG, F32), jnp.zeros((tq, 1), F32), jnp.zeros((tq, LANE), F32)) for _ in range(2))
        carry = lax.fori_loop(0, qi, lambda j, c: block(j, c, False), init)
        (m0, l0, acc0), (m1, l1, acc1) = block(qi, carry, True)
        o_ref[...] = jnp.where(lane < NOPE, acc0 / l0, acc1 / l1).astype(o_ref.dtype)
        lse_ref[...] = jnp.where(lane < NOPE, m0 + jnp.log(l0), m1 + jnp.log(l1))

    return _pcall(body, name=name, grid=(nb, npair, nq),
                  in_specs=[pl.BlockSpec((tq, 2 * LANE), lambda b, p, i: (b * nq + i, p)),
                            pl.BlockSpec((S, 2 * LANE), lambda b, p, i: (b, p)),
                            pl.BlockSpec((S, LANE), lambda b, p, i: (b, p))],
                  out_specs=[pl.BlockSpec((tq, LANE), lambda b, p, i: (b * nq + i, p)),
                             pl.BlockSpec((tq, LANE), lambda b, p, i: (b * nq + i, p))],
                  out_shape=[_sds((T, NH * NOPE), MX), _sds((T, NH * NOPE), F32)], comm=comm)(q, k, v)


def _attn_bwd(q, k, v, o, lse, do, *, nb, name, tq=512, comm=None):
    T = q.shape[0]
    S = T // nb
    tq = _pick(S, tq)
    nq = S // tq
    tk = tq
    npair = NH // 2

    def body(q_ref, k_ref, v_ref, o_ref, lse_ref, do_ref, dq_ref, dk_ref, dv_ref, delta_ref):
        lane = lax.broadcasted_iota(jnp.int32, (tq, LANE), 1)
        first = lane < NOPE
        dq_ref[...] = jnp.zeros_like(dq_ref)

        def delta_step(qi, _):
            q0 = pl.multiple_of(qi * tq, tq)
            prod = do_ref[pl.ds(q0, tq), :] * o_ref[pl.ds(q0, tq), :].astype(F32)
            d0 = jnp.sum(jnp.where(first, prod, 0.0), axis=-1, keepdims=True)
            d1 = jnp.sum(jnp.where(first, 0.0, prod), axis=-1, keepdims=True)
            delta_ref[pl.ds(q0, tq), :] = jnp.where(first, d0, d1)
            return 0

        lax.fori_loop(0, nq, delta_step, 0)

        def kv_step(kj, _):
            k0 = pl.multiple_of(kj * tk, tk)
            kbs = [k_ref[pl.ds(k0, tk), hh * LANE:(hh + 1) * LANE] for hh in range(2)]
            vb = v_ref[pl.ds(k0, tk), :]

            def q_block(qi, carry, diagonal):
                dk0, dk1, dv = carry
                dks = [dk0, dk1]
                q0 = pl.multiple_of(qi * tq, tq)
                dov = do_ref[pl.ds(q0, tq), :]
                lse_v = lse_ref[pl.ds(q0, tq), :]
                delta_v = delta_ref[pl.ds(q0, tq), :]
                for hh in range(2):
                    qh = q_ref[pl.ds(q0, tq), hh * LANE:(hh + 1) * LANE]
                    dob = jnp.where(first if hh == 0 else jnp.logical_not(first), dov, 0.0).astype(MX)
                    s = lax.dot_general(qh, kbs[hh], (((1,), (1,)), ((), ())), preferred_element_type=F32) * ATTN_SCALE
                    p = jnp.exp(s - lse_v[:, hh * NOPE:hh * NOPE + 1])
                    if diagonal:
                        p = jnp.where(_lower_triangle(tq), p, 0.0)
                    dp = lax.dot_general(dob, vb, (((1,), (1,)), ((), ())), preferred_element_type=F32)
                    ds = (p * (dp - delta_v[:, hh * NOPE:hh * NOPE + 1]) * ATTN_SCALE).astype(MX)
                    dks[hh] = dks[hh] + lax.dot_general(ds, qh, (((0,), (0,)), ((), ())), preferred_element_type=F32)
                    dv = dv + lax.dot_general(p.astype(MX), dob, (((0,), (0,)), ((), ())), preferred_element_type=F32)
                    dq_ref[pl.ds(q0, tq), hh * LANE:(hh + 1) * LANE] += jnp.dot(ds, kbs[hh], preferred_element_type=F32)
                return dks[0], dks[1], dv

            zero = jnp.zeros((tk, LANE), F32)
            carry = q_block(kj, (zero, zero, zero), True)
            dk0, dk1, dv = lax.fori_loop(kj + 1, nq, lambda qi, c: q_block(qi, c, False), carry)
            dk_ref[pl.ds(k0, tk), 0:LANE] = dk0
            dk_ref[pl.ds(k0, tk), LANE:2 * LANE] = dk1
            dv_ref[pl.ds(k0, tk), :] = dv
            return 0

        lax.fori_loop(0, nq, kv_step, 0)

    pair256 = pl.BlockSpec((S, 2 * LANE), lambda b, p: (b, p))
    pair128 = pl.BlockSpec((S, LANE), lambda b, p: (b, p))
    return _pcall(body, name=name, grid=(nb, npair),
                  in_specs=[pair256, pair256, pair128, pair128, pair128, pair128],
                  out_specs=[pair256, pair256, pair128],
                  out_shape=[_sds((T, NH * LANE), F32), _sds((T, NH * LANE), F32), _sds((T, NH * NOPE), F32)],
                  scratch=[pltpu.VMEM((S, LANE), F32)], comm=comm)(q, k, v, o, lse, do)


def _run(plan, fn, *args, name, **kw):
    rider = plan.rider(name)
    if rider is None:
        return fn(*args, name=name, **kw)
    outs, landed = fn(*args, name=name, comm=rider, **kw)
    plan.landed(name, landed)
    return outs


def _layer_fwd_bwd(x, tgt, mod3, cos, sin, plan, P):
    nb = mod3.shape[0]
    W = plan.W
    h1, gu1, a1 = _run(plan, _ffn_up_first, x, mod3, P["norm_ffn1"], W["ffn1_in"], sub=0, name="ffn1_up_a")
    gu1, a1 = _run(plan, _ffn_up_second, h1, W["ffn1_in"], gu1, a1, name="ffn1_up_b")
    f1, x1, h2 = _run(plan, _mm_resid_norm, a1, W["ffn1_out"], x, mod3, P["norm_mix"], sub=1, coef=0.5, name="ffn1_out")
    z = _run(plan, _mm, h2, W["w_in"], mode="nn", out_dtype=MX, name="mix_in", tn=1664)
    pooled, mixed, scaled = _pool_fwd(z, W["pool_grp"], P["pool_scale"], nb=nb, name="pool_fwd")
    br_pool = _mm(scaled, W["pool_proj"], mode="nn", out_dtype=MX, name="pool_proj")
    qn, kvn, q, k, v = _mla_in_fwd(z, W["q_up"], W["kv_up"], P["q_a_norm"], P["kv_a_norm"], cos, sin, P["q_gain"], P["k_gain"],
                                   name="mla_in_fwd")
    attn, lse = _run(plan, _attn_fwd, q, k, v, nb=nb, name="attn_fwd")
    br_mla, merged, mo, x2, h3 = _mix_out_fwd(z, br_pool, attn, W["mla_proj"], W["w_out"], x1, mod3, P["norm_ffn2"], sub=2,
                                              name="mix_out", tm=256)
    gu2, a2 = _ffn_up(h3, W["ffn2_in"], name="ffn2_up")
    dy, df2, st_fin, loss = _mm_loss(a2, W["ffn2_out"], x2, tgt, mod3, name="ffn2_out_loss")

    plan.grad("ffn2_out", _dw(a2, df2, name="d_ffn2_out"))
    dgu2 = _ffn_dact(df2, gu2, W["ffn2_out"], name="ffn2_dact")
    plan.grad("ffn2_in", _ffn_dw_in(h3, dgu2, name="d_ffn2_in"))
    dx2, dmo, st3 = _run(plan, _ffn_dh_norm, dgu2, W["ffn2_in"], x2, dy, mo, mod3, P["norm_ffn2"], sub=2, coef=1.0,
                         name="d_ffn2_h")
    plan.grad("w_out", _dw(merged, dmo, name="d_mix_out"))
    dbr_pool, dbr_mla, dgates, dattn = _mix_out_bwd(dmo, W["w_out"], z, br_pool, br_mla, W["mla_proj"], name="mix_out_bwd", tm=256)
    plan.grad("pool_proj", _dw(scaled, dbr_pool, name="d_pool_proj"))
    dscaled = _mm(dbr_pool, W["pool_proj"], mode="nt", out_dtype=F32, name="d_pool_scaled")
    du_pool, d_pool_grp, d_pool_scale = _pool_bwd(dscaled, mixed, pooled, W["pool_grp"], P["pool_scale"], nb=nb, name="pool_bwd")
    plan.grad("mla_proj", _dw(attn, dbr_mla, name="d_mla_proj"))
    dq, dk, dv = _run(plan, _attn_bwd, q, k, v, attn, lse, dattn, nb=nb, name="attn_bwd")
    dqp, dkvp, dkr, dql, dkvl, st_prep, st_q, st_kv = _mla_in_bwd(
        dq, dk, dv, z, W["q_up"], W["kv_up"], P["q_a_norm"], P["kv_a_norm"], cos, sin, P["q_gain"], P["k_gain"], nb=nb,
        name="mla_in_bwd")
    plan.grad("q_up", _dw(qn, dqp, name="d_q_up"))
    plan.grad("kv_up", _dw(kvn, dkvp, name="d_kv_up"))
    dz = jnp.concatenate([du_pool, dql, dkvl, dkr, dgates], axis=1)
    plan.grad("w_in", _run(plan, _dw, h2, dz, name="d_mix_in", tm=256, tn=1664))
    dx1, df1, st2 = _run(plan, _mm_nt_norm, dz, W["w_in"], x1, dx2, f1, mod3, P["norm_mix"], sub=1, coef=0.5, name="d_mix_h")
    plan.grad("ffn1_out", _run(plan, _dw, a1, df1, name="d_ffn1_out"))
    dgu1 = _run(plan, _ffn_dact, df1, gu1, W["ffn1_out"], name="ffn1_dact")
    half = x.shape[1] // 2
    plan.grad("ffn1_in@0", _run(plan, _ffn_dw_in, h1, dgu1, rows=(0, half), name="d_ffn1_in_a"))
    plan.grad("ffn1_in@1", _run(plan, _ffn_dw_in, h1, dgu1, rows=(half, half), name="d_ffn1_in_b"))
    dh1 = _run(plan, _ffn_dh, dgu1, W["ffn1_in"], name="d_ffn1_h")
    grad_x, st1 = _run(plan, _bwd_block, x, dh1, dx1, None, mod3, P["norm_ffn1"], sub=0, coef=0.0, name="bwd_norm1")

    return loss, grad_x, (st1, st2, st3, st_fin, st_q, st_kv, st_prep, d_pool_scale), d_pool_grp


def _w_in_to_kernel(w):
    k = w.shape[0]
    zeros = lambda n: jnp.zeros((k, n), w.dtype)
    return jnp.concatenate([w[:, 0:1152], zeros(KR_LANE), w[:, 1152:1184], zeros(LANE - KR_LANE - ROPE), w[:, 1184:]], axis=1)


def _w_in_from_kernel(g):
    return jnp.concatenate([g[:, 0:1152], g[:, Z_KR + KR_LANE:Z_KR + KR_LANE + ROPE], g[:, Z_GP:]], axis=1)


def _q_up_to_kernel(w):
    k = w.shape[0]
    return jnp.pad(w.reshape(k, NH, NOPE + ROPE), ((0, 0), (0, 0), (0, LANE - NOPE - ROPE))).reshape(k, NH * LANE)


def _q_up_from_kernel(g):
    k = g.shape[0]
    return g.reshape(k, NH, LANE)[:, :, :NOPE + ROPE].reshape(k, NH * (NOPE + ROPE))


def _kv_up_to_kernel(w):
    k = w.shape[0]
    w3 = w.reshape(k, NH, 2 * NOPE)
    kpart = jnp.pad(w3[:, :, :NOPE], ((0, 0), (0, 0), (0, LANE - NOPE))).reshape(k, NH * LANE)
    return jnp.concatenate([kpart, w3[:, :, NOPE:].reshape(k, NH * NOPE)], axis=1)


def _kv_up_from_kernel(g):
    k = g.shape[0]
    kpart = g[:, :NH * LANE].reshape(k, NH, LANE)[:, :, :NOPE]
    vpart = g[:, NH * LANE:].reshape(k, NH, NOPE)
    return jnp.concatenate([kpart, vpart], axis=2).reshape(k, NH * 2 * NOPE)


def _gain_slab(nope, rope):
    return jnp.concatenate([nope, rope, jnp.zeros((1, LANE - NOPE - ROPE), nope.dtype)], axis=1)


def _rope_tables(positions):
    inv_freq = 10000.0 ** (-jnp.arange(0, ROPE, 2, dtype=F32) / ROPE)
    ang = positions.astype(F32)[:, None] * inv_freq
    ang = jnp.concatenate([ang, ang], axis=-1)
    t = positions.shape[0]
    cos = jnp.concatenate([jnp.ones((t, KR_LANE), F32), jnp.cos(ang), jnp.ones((t, LANE - KR_LANE - ROPE), F32)], axis=1)
    sin = jnp.concatenate([jnp.zeros((t, KR_LANE), F32), jnp.sin(ang), jnp.zeros((t, LANE - KR_LANE - ROPE), F32)], axis=1)
    return cos, sin


def _coords():
    return lax.axis_index("x"), lax.axis_index("y"), lax.axis_index("c")


HBM_SPEC = pl.BlockSpec(memory_space=pl.ANY)
VMEM_SPEC = pl.BlockSpec(memory_space=pltpu.VMEM)


CHIP_RELS = ((1, 0), (0, 1), (1, 1))


def _gather_comm(shards):
    n = len(shards)

    def ici(ins, outs, sems, a, j, x, y, cc):
        half = ins[a].shape[0] // 2
        mine = pl.ds(cc * half, half)
        dx, dy = CHIP_RELS[j]
        return pltpu.make_async_remote_copy(src_ref=ins[a].at[mine], dst_ref=outs[a].at[2 * x + y, mine],
                                            send_sem=sems[0].at[a, j], recv_sem=sems[1].at[a, j],
                                            device_id=(x ^ dx, y ^ dy, cc), device_id_type=MESH)

    def d2d(ins, outs, sems, a, j, x, y, cc, half_of):
        half = ins[a].shape[0] // 2
        dx, dy = CHIP_RELS[j]
        landed = outs[a].at[2 * (x ^ dx) + (y ^ dy), pl.ds(half_of * half, half)]
        return pltpu.make_async_remote_copy(src_ref=landed, dst_ref=landed, send_sem=sems[2].at[a, j], recv_sem=sems[3].at[a, j],
                                            device_id=(x, y, 1 - cc), device_id_type=MESH)

    def own(ins, outs, sems, a, x, y):
        return pltpu.make_async_copy(ins[a], outs[a].at[2 * x + y], sems[4].at[a])

    def start(ins, outs, sems):
        x, y, cc = _coords()
        for a in range(n):
            own(ins, outs, sems, a, x, y).start()
            for j in range(3):
                ici(ins, outs, sems, a, j, x, y, cc).start()

    def finish(ins, outs, sems):
        x, y, cc = _coords()
        for a in range(n):
            for j in range(3):
                ici(ins, outs, sems, a, j, x, y, cc).wait_recv()
                d2d(ins, outs, sems, a, j, x, y, cc, cc).start()
        for a in range(n):
            for j in range(3):
                d2d(ins, outs, sems, a, j, x, y, cc, 1 - cc).wait_recv()
        for a in range(n):
            for j in range(3):
                ici(ins, outs, sems, a, j, x, y, cc).wait_send()
                d2d(ins, outs, sems, a, j, x, y, cc, cc).wait_send()
            own(ins, outs, sems, a, x, y).wait()

    dma = pltpu.SemaphoreType.DMA
    return _Comm(shards, [_sds((N_CHIP,) + s.shape, s.dtype) for s in shards],
                 [dma((n, 3)), dma((n, 3)), dma((n, 3)), dma((n, 3)), dma((n,))], start, finish)


def _swap_comm(parts):
    n = len(parts)

    def copy(ins, outs, sems, a):
        x, y, cc = _coords()
        half = ins[a].shape[1] // 2
        return pltpu.make_async_remote_copy(src_ref=ins[a].at[:, pl.ds((1 - cc) * half, half)], dst_ref=outs[a],
                                            send_sem=sems[0].at[a], recv_sem=sems[1].at[a], device_id=(x, y, 1 - cc),
                                            device_id_type=MESH)

    def start(ins, outs, sems):
        for a in range(n):
            copy(ins, outs, sems, a).start()

    def finish(ins, outs, sems):
        for a in range(n):
            copy(ins, outs, sems, a).wait()

    dma = pltpu.SemaphoreType.DMA
    return _Comm(parts, [_sds((p.shape[0], p.shape[1] // 2, p.shape[2]), p.dtype) for p in parts], [dma((n,)), dma((n,))],
                 start, finish)


def _add_half(full, other, cidx, *, name):
    nch, r, c = full.shape
    half = r // 2
    tr = _pick_rows(half)
    nbk = half // tr
    grid_spec = pltpu.PrefetchScalarGridSpec(
        num_scalar_prefetch=1, grid=(nch, nbk),
        in_specs=[pl.BlockSpec((1, tr, c), lambda j, i, cref: (j, cref[0] * nbk + i, 0)),
                  pl.BlockSpec((1, tr, c), lambda j, i, cref: (j, i, 0))],
        out_specs=pl.BlockSpec((1, tr, c), lambda j, i, cref: (j, i, 0)))

    def body(cref, a_ref, b_ref, o_ref):
        o_ref[...] = (a_ref[...] + b_ref[...]).astype(o_ref.dtype)

    return _pcall(body, name=name, out_shape=_sds((nch, half, c), MX), grid_spec=grid_spec)(cidx, full, other)


def _pick_rows(rows, target=512):
    best = None
    for t in range(16, min(rows, target) + 1, 16):
        if rows % t == 0:
            best = t
    return rows if best is None else best


def _exchange_comm(parts):
    n = len(parts)

    def send(ins, outs, sems, a, j, x, y, cc):
        dx, dy = CHIP_RELS[j]
        return pltpu.make_async_remote_copy(src_ref=ins[a].at[2 * (x ^ dx) + (y ^ dy)], dst_ref=outs[a].at[2 * x + y],
                                            send_sem=sems[0].at[a, j], recv_sem=sems[1].at[a, j],
                                            device_id=(x ^ dx, y ^ dy, cc), device_id_type=MESH)

    def landing(ins, outs, sems, a, j, x, y, cc):
        dx, dy = CHIP_RELS[j]
        peer_chip = 2 * (x ^ dx) + (y ^ dy)
        return pltpu.make_async_remote_copy(src_ref=ins[a].at[peer_chip], dst_ref=outs[a].at[peer_chip], send_sem=sems[0].at[a, j],
                                            recv_sem=sems[1].at[a, j], device_id=(x, y, cc), device_id_type=MESH)

    def own(ins, outs, sems, a, x, y):
        return pltpu.make_async_copy(ins[a].at[2 * x + y], outs[a].at[2 * x + y], sems[2].at[a])

    def start(ins, outs, sems):
        x, y, cc = _coords()
        for a in range(n):
            own(ins, outs, sems, a, x, y).start()
            for j in range(3):
                send(ins, outs, sems, a, j, x, y, cc).start()

    def finish(ins, outs, sems):
        x, y, cc = _coords()
        for a in range(n):
            for j in range(3):
                landing(ins, outs, sems, a, j, x, y, cc).wait_recv()
        for a in range(n):
            for j in range(3):
                send(ins, outs, sems, a, j, x, y, cc).wait_send()
            own(ins, outs, sems, a, x, y).wait()

    dma = pltpu.SemaphoreType.DMA
    return _Comm(parts, [_sds(p.shape, p.dtype) for p in parts], [dma((n, 3)), dma((n, 3)), dma((n,))], start, finish)


def _sum_chips(q, cidx, *, name):
    nch, h, c = q.shape
    tr = _pick_rows(h)
    nbk = h // tr
    grid_spec = pltpu.PrefetchScalarGridSpec(
        num_scalar_prefetch=1, grid=(nbk,),
        in_specs=[pl.BlockSpec((nch, tr, c), lambda i, cref: (0, i, 0))],
        out_specs=pl.BlockSpec((tr, c), lambda i, cref: (cref[0] * nbk + i, 0)))

    def body(cref, q_ref, o_ref):
        acc = q_ref[0].astype(F32) + q_ref[1].astype(F32)
        acc = acc + q_ref[2].astype(F32)
        o_ref[...] = acc + q_ref[3].astype(F32)

    return _pcall(body, name=name, out_shape=_sds((2 * h, c), F32), grid_spec=grid_spec)(cidx, q)


def _join_comm(fulls):
    n = len(fulls)

    def half_copy(outs, sems, a, which):
        x, y, cc = _coords()
        h = outs[a].shape[0] // 2
        rows = outs[a].at[pl.ds((cc if which == 0 else 1 - cc) * h, h)]
        return pltpu.make_async_remote_copy(src_ref=rows, dst_ref=rows, send_sem=sems[0].at[a], recv_sem=sems[1].at[a],
                                            device_id=(x, y, 1 - cc), device_id_type=MESH)

    def start(ins, outs, sems):
        for a in range(n):
            half_copy(outs, sems, a, 0).start()

    def finish(ins, outs, sems):
        for a in range(n):
            half_copy(outs, sems, a, 1).wait_recv()
        for a in range(n):
            half_copy(outs, sems, a, 0).wait_send()

    dma = pltpu.SemaphoreType.DMA
    return _Comm(fulls, [_sds(p.shape, p.dtype) for p in fulls], [dma((n,)), dma((n,))], start, finish,
                 aliases={a: a for a in range(n)})


def _ada_prologue(c, w, b, *, name, comm=None):
    nb, dm = c.shape
    n = w.shape[1]

    def body(c_ref, w_ref, b_ref, call_ref, land_ref, cpad, mod_s, g_send, g_recv, m_send, m_recv):
        x, y, cc = _coords()
        me = 4 * x + 2 * y + cc
        chip = 2 * x + y
        cpad[...] = jnp.zeros_like(cpad)
        cpad[0:nb, :] = c_ref[...]
        copies = []
        for kk in range(1, N_DEV):
            peer = (x ^ (kk >> 2), y ^ ((kk >> 1) & 1), cc ^ (kk & 1))
            cp = pltpu.make_async_remote_copy(src_ref=cpad, dst_ref=call_ref.at[me], send_sem=g_send.at[kk - 1],
                                              recv_sem=g_recv.at[kk - 1], device_id=peer, device_id_type=MESH)
            cp.start()
            copies.append(cp)
        call_ref[me] = cpad[...]
        for kk in range(1, N_DEV):
            pltpu.make_async_remote_copy(src_ref=cpad, dst_ref=call_ref.at[me ^ kk], send_sem=g_send.at[kk - 1],
                                         recv_sem=g_recv.at[kk - 1], device_id=(x, y, cc), device_id_type=MESH).wait_recv()
        cv = call_ref[...].reshape(N_DEV * SUBLANE, dm)
        act = (cv * _sigmoid(cv)).astype(MX)
        mod = jnp.dot(act, w_ref[...].astype(MX), preferred_element_type=F32) + b_ref[...]
        mod_s[...] = mod.reshape(N_DEV, SUBLANE, n)
        for j, (dx, dy) in enumerate(CHIP_RELS):
            cp = pltpu.make_async_remote_copy(src_ref=mod_s.at[4 * (x ^ dx) + 2 * (y ^ dy) + cc], dst_ref=land_ref.at[chip],
                                              send_sem=m_send.at[j], recv_sem=m_recv.at[j],
                                              device_id=(x ^ dx, y ^ dy, cc), device_id_type=MESH)
            cp.start()
            copies.append(cp)
        land_ref[chip] = mod_s[me]
        for j, (dx, dy) in enumerate(CHIP_RELS):
            pltpu.make_async_remote_copy(src_ref=mod_s.at[me], dst_ref=land_ref.at[2 * (x ^ dx) + (y ^ dy)], send_sem=m_send.at[j],
                                         recv_sem=m_recv.at[j], device_id=(x, y, cc), device_id_type=MESH).wait_recv()
        for cp in copies:
            cp.wait_send()

    dma = pltpu.SemaphoreType.DMA
    return _pcall(body, name=name, in_specs=[VMEM_SPEC] * 3, out_specs=[VMEM_SPEC] * 2,
                  out_shape=[_sds((N_DEV, SUBLANE, dm), F32), _sds((N_CHIP, SUBLANE, n), F32)],
                  scratch=[pltpu.VMEM((SUBLANE, dm), F32), pltpu.VMEM((N_DEV, SUBLANE, n), F32),
                           dma((N_DEV - 1,)), dma((N_DEV - 1,)), dma((3,)), dma((3,))], comm=comm)(c, w, b)


def _ada_bwd(c_all, dmod_cols, *, name):
    m, kdim = c_all.shape
    n = dmod_cols.shape[1]
    tn = _pick(n, 1152)

    def body(c_ref, d_ref, o_ref):
        cv = c_ref[...]
        act = (cv * _sigmoid(cv)).astype(MX)
        o_ref[...] = lax.dot_general(act, d_ref[...].astype(MX), (((0,), (0,)), ((), ())), preferred_element_type=F32)

    return _pcall(body, name=name, out_shape=_sds((kdim, n), F32), grid=(n // tn,),
                  in_specs=[pl.BlockSpec((m, kdim), lambda j: (0, 0)), pl.BlockSpec((m, tn), lambda j: (0, j))],
                  out_specs=pl.BlockSpec((kdim, tn), lambda j: (0, j)))(c_all, dmod_cols)


SLAB_W = 1024
RED_ROWS = 8


LOSS_LANE = SLAB_W - LANE


def _pack_stats(st1, st2, st3, st_fin, st_q, st_kv, st_prep, d_pool_scale, loss8, *, name):
    nb = st1.shape[0]
    dm_rows = -(-9 * nb // SUBLANE) * SUBLANE

    def body(s1, s2, s3, sf, sq, skv, sp, sps, loss_ref, o_ref):
        o_ref[...] = jnp.zeros_like(o_ref)
        for s in range(nb):
            rows = [s1[s, 0:1, :], s1[s, 1:2, :], s2[s, 3:4, :], s2[s, 0:1, :], s2[s, 1:2, :], s3[s, 3:4, :], s3[s, 0:1, :],
                    s3[s, 1:2, :], sf[s, 0:1, :]]
            for k, row in enumerate(rows):
                o_ref[9 * s + k:9 * s + k + 1, :] = row

        def over_seq(ref, r):
            acc = ref[0, r:r + 1, :]
            for s in range(1, nb):
                acc = acc + ref[s, r:r + 1, :]
            return acc

        o_ref[dm_rows + 0:dm_rows + 1, :] = over_seq(s1, 2)
        o_ref[dm_rows + 1:dm_rows + 2, :] = over_seq(s2, 2)
        o_ref[dm_rows + 2:dm_rows + 3, :] = over_seq(s3, 2)
        o_ref[dm_rows + 3:dm_rows + 4, 0:POOL_W] = over_seq(sps, 0)
        o_ref[dm_rows + 4:dm_rows + 5, 0:QL] = over_seq(sq, 0)
        o_ref[dm_rows + 5:dm_rows + 6, 0:KVL] = over_seq(skv, 0)
        o_ref[dm_rows + 6:dm_rows + 7, 0:LANE] = over_seq(sp, 0)
        o_ref[dm_rows + 7:dm_rows + 8, 0:LANE] = over_seq(sp, 1)
        o_ref[dm_rows + 7:dm_rows + 8, LOSS_LANE:] = loss_ref[0:1, :]

    return _pcall(body, name=name, out_shape=_sds((dm_rows + RED_ROWS, SLAB_W), F32), in_specs=[VMEM_SPEC] * 9,
                  out_specs=VMEM_SPEC)(st1, st2, st3, st_fin, st_q, st_kv, st_prep, d_pool_scale, loss8)


def _small_allreduce(slab, pool, *, name, comm=None):
    rows, w = slab.shape
    dm = rows - RED_ROWS
    prow, pw = pool.shape
    crow = RED_ROWS + 2 * dm

    def body(slab_ref, pool_ref, red_ref, dmod_ref, ptot_ref, sib_slab, sib_pool, chip_slab, chip_pool, land_slab, land_pool,
             a_send, a_recv, b_send, b_recv):
        x, y, cc = _coords()
        chip = 2 * x + y
        sib = (x, y, 1 - cc)
        to_sib = [pltpu.make_async_remote_copy(src_ref=slab_ref, dst_ref=sib_slab, send_sem=a_send.at[0], recv_sem=a_recv.at[0],
                                               device_id=sib, device_id_type=MESH),
                  pltpu.make_async_remote_copy(src_ref=pool_ref, dst_ref=sib_pool, send_sem=a_send.at[1], recv_sem=a_recv.at[1],
                                               device_id=sib, device_id_type=MESH)]
        for cp in to_sib:
            cp.start()
        for cp in to_sib:
            cp.wait()
        mine_dm, theirs_dm = slab_ref[0:dm, :], sib_slab[0:dm, :]
        chip_slab[0:RED_ROWS, :] = slab_ref[dm:, :] + sib_slab[dm:, :]
        chip_slab[RED_ROWS:RED_ROWS + dm, :] = jnp.where(cc == 0, mine_dm, theirs_dm)
        chip_slab[RED_ROWS + dm:, :] = jnp.where(cc == 0, theirs_dm, mine_dm)
        chip_pool[...] = pool_ref[...] + sib_pool[...]

        sends = []
        for j, (dx, dy) in enumerate(CHIP_RELS):
            peer = (x ^ dx, y ^ dy, cc)
            sends.append(pltpu.make_async_remote_copy(src_ref=chip_slab, dst_ref=land_slab.at[chip], send_sem=b_send.at[j, 0],
                                                      recv_sem=b_recv.at[j, 0], device_id=peer, device_id_type=MESH))
            sends.append(pltpu.make_async_remote_copy(src_ref=chip_pool, dst_ref=land_pool.at[chip], send_sem=b_send.at[j, 1],
                                                      recv_sem=b_recv.at[j, 1], device_id=peer, device_id_type=MESH))
        for cp in sends:
            cp.start()
        land_slab[chip] = chip_slab[...]
        land_pool[chip] = chip_pool[...]
        for j, (dx, dy) in enumerate(CHIP_RELS):
            peer_chip = 2 * (x ^ dx) + (y ^ dy)
            pltpu.make_async_remote_copy(src_ref=chip_slab, dst_ref=land_slab.at[peer_chip], send_sem=b_send.at[j, 0],
                                         recv_sem=b_recv.at[j, 0], device_id=(x, y, cc), device_id_type=MESH).wait_recv()
            pltpu.make_async_remote_copy(src_ref=chip_pool, dst_ref=land_pool.at[peer_chip], send_sem=b_send.at[j, 1],
                                         recv_sem=b_recv.at[j, 1], device_id=(x, y, cc), device_id_type=MESH).wait_recv()
        red = land_slab[0, 0:RED_ROWS, :]
        ptot = land_pool[0]
        for ch in range(1, N_CHIP):
            red = red + land_slab[ch, 0:RED_ROWS, :]
            ptot = ptot + land_pool[ch]
        red_ref[...] = red
        ptot_ref[...] = ptot
        for ch in range(N_CHIP):
            dmod_ref[2 * dm * ch:2 * dm * (ch + 1), :] = land_slab[ch, RED_ROWS:, :]
        for cp in sends:
            cp.wait_send()

    dma = pltpu.SemaphoreType.DMA
    return _pcall(body, name=name, in_specs=[VMEM_SPEC, VMEM_SPEC], out_specs=[VMEM_SPEC] * 3,
                  out_shape=[_sds((RED_ROWS, w), F32), _sds((N_DEV * dm, w), F32), _sds((prow, pw), F32)],
                  scratch=[pltpu.VMEM((rows, w), F32), pltpu.VMEM((prow, pw), F32), pltpu.VMEM((crow, w), F32),
                           pltpu.VMEM((prow, pw), F32), pltpu.VMEM((N_CHIP, crow, w), F32), pltpu.VMEM((N_CHIP, prow, pw), F32),
                           dma((2,)), dma((2,)), dma((3, 2)), dma((3, 2))], comm=comm)(slab, pool)


def _adamw_math(w, g, m, v):
    mn = ADAM_B1 * m + (1.0 - ADAM_B1) * g
    vn = ADAM_B2 * v + (1.0 - ADAM_B2) * (g * g)
    bc1 = 1.0 / (1.0 - ADAM_B1 ** ADAM_STEP)
    bc2 = 1.0 / (1.0 - ADAM_B2 ** ADAM_STEP)
    return -ADAM_LR * ((mn * bc1) / (jnp.sqrt(vn * bc2) + ADAM_EPS) + ADAM_WD * w), mn, vn


def _small_update(red, dmod_all, pool_total, nb, params, *, name):
    names = list(SMALL)
    dm = dmod_all.shape[0] // N_DEV

    def grad_of(nm, red_ref, dmod_ref, ptot_ref):
        if nm == "b_ada":
            acc = None
            for d in range(N_DEV):
                for s in range(nb):
                    blk = dmod_ref[d * dm + 9 * s:d * dm + 9 * s + 9, :]
                    acc = blk if acc is None else acc + blk
            return jnp.concatenate([acc[k:k + 1, :] for k in range(9)], axis=1)
        if nm == "pool_grp":
            return ptot_ref[...]
        row, lo, n = {"norm_ffn1": (0, 0, D), "norm_mix": (1, 0, D), "norm_ffn2": (2, 0, D), "pool_scale": (3, 0, POOL_W),
                      "q_a_norm": (4, 0, QL), "kv_a_norm": (5, 0, KVL), "q_norm_nope": (6, 0, NOPE),
                      "q_norm_rope": (6, NOPE, ROPE), "k_norm_nope": (7, 0, NOPE), "k_norm_rope": (7, KR_LANE, ROPE)}[nm]
        return red_ref[row:row + 1, lo:lo + n]

    def body(*refs):
        red_ref, dmod_ref, ptot_ref = refs[:3]
        ins = refs[3:3 + 3 * len(names)]
        outs = refs[3 + 3 * len(names):]
        outs[4 * len(names)][...] = red_ref[RED_ROWS - 1:RED_ROWS, LOSS_LANE:]
        for i, nm in enumerate(names):
            g = grad_of(nm, red_ref, dmod_ref, ptot_ref)
            d, mn, vn = _adamw_math(ins[3 * i][...], g, ins[3 * i + 1][...], ins[3 * i + 2][...])
            outs[4 * i][...] = g
            outs[4 * i + 1][...] = d
            outs[4 * i + 2][...] = mn
            outs[4 * i + 3][...] = vn

    flat_in = [a for nm in names for a in params[nm]]
    out_shape = [_sds(params[nm][0].shape, F32) for nm in names for _ in range(4)] + [_sds((1, LANE), F32)]
    res = _pcall(body, name=name, in_specs=[VMEM_SPEC] * (3 + len(flat_in)), out_specs=[VMEM_SPEC] * len(out_shape),
                 out_shape=out_shape)(red, dmod_all, pool_total, *flat_in)
    return {nm: tuple(res[4 * i:4 * i + 4]) for i, nm in enumerate(names)}, res[-1]


def _adamw(w, g, m, v, *, name, comm=None):
    r, c = w.shape
    tr = _pick_rows(r, 256)
    tc = c if tr < r else _pick(c, 256)
    spec = pl.BlockSpec((tr, tc), lambda i, j: (i, j))
    bc1 = 1.0 / (1.0 - ADAM_B1 ** ADAM_STEP)
    bc2 = 1.0 / (1.0 - ADAM_B2 ** ADAM_STEP)

    def body(w_ref, g_ref, m_ref, v_ref, d_ref, mo_ref, vo_ref):
        gv = g_ref[...]
        mn = ADAM_B1 * m_ref[...] + (1.0 - ADAM_B1) * gv
        vn = ADAM_B2 * v_ref[...] + (1.0 - ADAM_B2) * (gv * gv)
        mo_ref[...] = mn
        vo_ref[...] = vn
        d_ref[...] = -ADAM_LR * ((mn * bc1) / (jnp.sqrt(vn * bc2) + ADAM_EPS) + ADAM_WD * w_ref[...])

    out = _sds((r, c), F32)
    return _pcall(body, name=name, out_shape=[out, out, out], grid=(r // tr, c // tc), in_specs=[spec] * 4, out_specs=[spec] * 3,
                  comm=comm)(w, g, m, v)


BIG = ("w_ffn1_in", "w_ffn1_out", "w_in", "w_pool_proj", "w_q_up", "w_kv_up", "w_mla_proj", "w_out", "w_ffn2_in", "w_ffn2_out")
ROW_SHARDED = ("w_ffn1_out", "w_out", "w_ffn2_out")
KERNEL_NAME = {"w_ffn1_in": "ffn1_in", "w_ffn1_out": "ffn1_out", "w_in": "w_in", "w_pool_proj": "pool_proj", "w_q_up": "q_up",
               "w_kv_up": "kv_up", "w_mla_proj": "mla_proj", "w_out": "w_out", "w_ffn2_in": "ffn2_in", "w_ffn2_out": "ffn2_out"}
WEIGHTS = ("w_ada", "b_ada", "norm_ffn1", "w_ffn1_in", "w_ffn1_out", "norm_mix", "w_in", "pool_grp", "pool_scale", "w_pool_proj",
           "q_a_norm", "w_q_up", "kv_a_norm", "w_kv_up", "q_norm_nope", "q_norm_rope", "k_norm_nope", "k_norm_rope", "w_mla_proj",
           "w_out", "norm_ffn2", "w_ffn2_in", "w_ffn2_out")
SMALL = ("b_ada", "norm_ffn1", "norm_mix", "pool_grp", "pool_scale", "q_a_norm", "kv_a_norm", "q_norm_nope", "q_norm_rope",
         "k_norm_nope", "k_norm_rope", "norm_ffn2")


def _assemble(name, stacked):
    if name in ROW_SHARDED:
        return stacked.reshape(stacked.shape[0] * stacked.shape[1], stacked.shape[2])
    return jnp.transpose(stacked, (1, 0, 2)).reshape(stacked.shape[1], stacked.shape[0] * stacked.shape[2])


def _split(name, full):
    if name in ROW_SHARDED:
        return full.reshape(N_CHIP, full.shape[0] // N_CHIP, full.shape[1])
    return jnp.transpose(full.reshape(full.shape[0], N_CHIP, full.shape[1] // N_CHIP), (1, 0, 2))


GU = ("w_ffn1_in", "w_ffn2_in")
NARROW = ("w_in", "w_q_up")


MIX_SMALL = ("w_out", "w_pool_proj", "w_mla_proj", "w_q_up", "w_kv_up")
RIDES = {
    "ada_prologue": (("gather", ("w_ffn1_in",)),),
    "ffn1_up_a": (("gather", ("w_ffn1_out",)),),
    "ffn1_up_b": (("gather", ("w_in",)),),
    "mix_in": (("gather", MIX_SMALL),),
    "attn_fwd": (("gather", ("w_ffn2_in", "w_ffn2_out")),),
    "d_ffn2_h": (("swap", ("w_ffn2_out", "w_ffn2_in")),),
    "attn_bwd": (("exchange", ("w_ffn2_out", "w_ffn2_in")),),
    "d_mix_in": (("swap", MIX_SMALL),),
    "d_mix_h": (("exchange", MIX_SMALL), ("swap", ("w_in",))),
    "ffn1_dact": (("exchange", ("w_in",)), ("swap", ("w_ffn1_out",))),
    "d_ffn1_in_a": (("exchange", ("w_ffn1_out",)),),
    "d_ffn1_in_b": (("swap", ("w_ffn1_in@0",)),),
    "d_ffn1_h": (("exchange", ("w_ffn1_in@0",)), ("swap", ("w_ffn1_in@1",))),
    "bwd_norm1": (("exchange", ("w_ffn1_in@1",)),),
    "small_allreduce": (("join", tuple(n for n in BIG if n != "w_ffn1_in") + ("w_ffn1_in@0", "w_ffn1_in@1")),),
}


def _kname(n):
    base, _, part = n.partition("@")
    return KERNEL_NAME[base] + ("@" + part if part else "")


def _both(comms):
    if len(comms) == 1:
        return comms[0]
    ins, outs, sems, aliases, spans = [], [], [], {}, []
    for c in comms:
        spans.append((len(ins), len(c.ins), len(outs), len(c.out_shapes), len(sems), len(c.sems)))
        aliases.update({len(ins) + i: len(outs) + o for i, o in c.aliases.items()})
        ins, outs, sems = ins + c.ins, outs + c.out_shapes, sems + c.sems

    def each(which):
        def run(i_, o_, s_):
            for c, (ia, ni, oa, no, sa, ns) in zip(comms, spans):
                getattr(c, which)(i_[ia:ia + ni], o_[oa:oa + no], s_[sa:sa + ns])
        return run

    return _Comm(ins, outs, sems, each("start"), each("finish"), aliases)


class _ExchangePlan:
    def __init__(self, shards, cidx):
        self.shards, self.cidx = shards, cidx
        self.W, self.G, self.parts, self.pre, self.reduced, self.joined = {}, {}, {}, {}, {}, {}

    def grad(self, key, g):
        self.G[key] = g

    def rider(self, name):
        comms = []
        for kind, names in RIDES.get(name, ()):
            if kind == "gather":
                comms.append(_gather_comm([self.shards[n] for n in names]))
            elif kind == "swap":
                for n in names:
                    self.parts[n] = self._stacked(n)
                comms.append(_swap_comm([self.parts[n] for n in names]))
            elif kind == "exchange":
                comms.append(_exchange_comm([self.pre[n] for n in names]))
            else:
                comms.append(_join_comm([self.reduced[n] for n in names]))
        return _both(comms) if comms else None

    def landed(self, name, outs):
        at = 0
        for kind, names in RIDES[name]:
            for n, o in zip(names, outs[at:at + len(names)]):
                if kind == "gather":
                    self.W[KERNEL_NAME[n]] = self._to_kernel(n, o)
                elif kind == "swap":
                    self.pre[n] = _add_half(self.parts[n], o, self.cidx, name="rs_add_" + _kname(n))
                elif kind == "exchange":
                    self.reduced[n] = _sum_chips(o, self.cidx, name="rs_sum_" + _kname(n))
                else:
                    self.joined[n] = o
            at += len(names)

    @staticmethod
    def _to_kernel(n, stacked):
        if n in GU:
            return stacked
        full = _assemble(n, stacked)
        return {"w_in": _w_in_to_kernel, "w_q_up": _q_up_to_kernel, "w_kv_up": _kv_up_to_kernel}.get(n, lambda w: w)(full)

    def _stacked(self, n):
        g = self.G[_kname(n)]
        if n.partition("@")[0] in GU:
            return g
        full = {"w_in": _w_in_from_kernel, "w_q_up": _q_up_from_kernel, "w_kv_up": _kv_up_from_kernel}.get(n, lambda w: w)(g)
        return _split(n, full)


def kernel(x, c, positions, w_ada, b_ada, norm_ffn1, w_ffn1_in, w_ffn1_out, norm_mix, w_in, pool_grp, pool_scale, w_pool_proj, q_a_norm, w_q_up, kv_a_norm, w_kv_up, q_norm_nope, q_norm_rope, k_norm_nope, k_norm_rope, w_mla_proj, w_out, norm_ffn2, w_ffn2_in, w_ffn2_out, loss_target, m_w_ada, m_b_ada, m_norm_ffn1, m_w_ffn1_in, m_w_ffn1_out, m_norm_mix, m_w_in, m_pool_grp, m_pool_scale, m_w_pool_proj, m_q_a_norm, m_w_q_up, m_kv_a_norm, m_w_kv_up, m_q_norm_nope, m_q_norm_rope, m_k_norm_nope, m_k_norm_rope, m_w_mla_proj, m_w_out, m_norm_ffn2, m_w_ffn2_in, m_w_ffn2_out, v_w_ada, v_b_ada, v_norm_ffn1, v_w_ffn1_in, v_w_ffn1_out, v_norm_mix, v_w_in, v_pool_grp, v_pool_scale, v_w_pool_proj, v_q_a_norm, v_w_q_up, v_kv_a_norm, v_w_kv_up, v_q_norm_nope, v_q_norm_rope, v_k_norm_nope, v_k_norm_rope, v_w_mla_proj, v_w_out, v_norm_ffn2, v_w_ffn2_in, v_w_ffn2_out):
    args = dict(locals())
    wts = {n: args[n][0] for n in WEIGHTS}
    mom = {n: args["m_" + n][0] for n in WEIGHTS}
    var = {n: args["v_" + n][0] for n in WEIGHTS}
    nb, seq, dm = x.shape
    tokens = nb * seq
    xi, yi, ci = _coords()
    chip = 2 * xi + yi

    cidx = ci.astype(jnp.int32).reshape(1)
    plan = _ExchangePlan({n: wts[n].astype(MX) for n in BIG}, cidx)
    plan.W["pool_grp"] = wts["pool_grp"].astype(MX)

    ncol = w_ada.shape[2]
    b_cols = lax.dynamic_slice_in_dim(wts["b_ada"].reshape(1, -1), chip * ncol, ncol, axis=1)
    c_slots, mod_slots = _run(plan, _ada_prologue, c, wts["w_ada"], b_cols, name="ada_prologue")
    c_all = c_slots[:, :nb].reshape(N_DEV * nb, dm)
    mod3 = jnp.transpose(mod_slots[:, :nb], (1, 0, 2)).reshape(nb, 9, dm)
    P = {"norm_ffn1": wts["norm_ffn1"].reshape(1, dm), "norm_mix": wts["norm_mix"].reshape(1, dm),
         "norm_ffn2": wts["norm_ffn2"].reshape(1, dm), "pool_scale": wts["pool_scale"].reshape(1, POOL_W),
         "q_a_norm": wts["q_a_norm"].reshape(1, QL), "kv_a_norm": wts["kv_a_norm"].reshape(1, KVL),
         "q_gain": _gain_slab(wts["q_norm_nope"].reshape(1, NOPE), wts["q_norm_rope"].reshape(1, ROPE)),
         "k_gain": _gain_slab(wts["k_norm_nope"].reshape(1, NOPE), wts["k_norm_rope"].reshape(1, ROPE))}
    cos, sin = _rope_tables(positions.reshape(tokens))

    loss8, grad_x, stats, d_pool_grp = _layer_fwd_bwd(x.reshape(tokens, dm), loss_target.reshape(tokens, dm), mod3, cos, sin, plan, P)

    slab = _pack_stats(*stats, loss8, name="pack_stats")
    red, dmod_rows, pool_total = _run(plan, _small_allreduce, slab, d_pool_grp.reshape(POOL_G * LANE, LANE), name="small_allreduce")
    grads = {n: plan.joined[n] for n in BIG if n != "w_ffn1_in"}
    grads["w_ffn1_in"] = jnp.concatenate([plan.joined["w_ffn1_in@0"], plan.joined["w_ffn1_in@1"]], axis=0)
    dm_rows = dmod_rows.shape[0] // N_DEV
    dmod_all = dmod_rows.reshape(N_DEV, dm_rows, SLAB_W)[:, :9 * nb].reshape(N_DEV * nb, 9 * dm)
    dmod_cols = lax.dynamic_slice_in_dim(dmod_all, chip * ncol, ncol, axis=1)
    grads["w_ada"] = _ada_bwd(c_all, dmod_cols, name="ada_bwd")

    delta, new_m, new_v = {}, {}, {}
    as2d = lambda a: a.reshape(POOL_G * LANE, LANE) if a.ndim == 4 else a.reshape(1, -1)
    upd, loss_row = _small_update(red, dmod_rows, pool_total, nb,
                                  {n: tuple(as2d(args[p + n]) for p in ("", "m_", "v_")) for n in SMALL}, name="small_update")
    loss = loss_row[0, 0]
    for n in SMALL:
        grads[n], delta[n], new_m[n], new_v[n] = upd[n]
    for n in ("w_ada",) + BIG:
        if n in NARROW:
            res = _adamw(wts[n].T, grads[n].T, mom[n].T, var[n].T, name="adamw_" + n)
            delta[n], new_m[n], new_v[n] = (r.T for r in res)
        else:
            delta[n], new_m[n], new_v[n] = _adamw(wts[n], grads[n], mom[n], var[n], name="adamw_" + n)

    def lead(a, n):
        return a.reshape((1,) + wts[n].shape)

    return (loss, grad_x.reshape(nb, seq, dm), *[lead(grads[n], n) for n in WEIGHTS], *[lead(delta[n], n) for n in WEIGHTS],
            *[lead(new_m[n], n) for n in WEIGHTS], *[lead(new_v[n], n) for n in WEIGHTS])
```

```python
import functools
import math

import jax
import jax.numpy as jnp
from jax import lax
from jax.experimental import pallas as pl
from jax.experimental.pallas import tpu as pltpu

F32 = jnp.float32
MX = jnp.bfloat16

D = 1024
DFF = 2816
NH = 8
POOL_W = 512
POOL_G = 4
QL = 384
KVL = 256
ROPE = 32
NOPE = 64
LANE = 128
SUBLANE = 8
EPS = 1e-6
ATTN_SCALE = 1.0 / math.sqrt(96.0)
NEG = -1e30

Z_UP, Z_QL, Z_KV, Z_KR, Z_GP, Z_GM, Z_W = 0, 512, 896, 1152, 1280, 2304, 3328
KR_LANE = 64
LAT_W = 1280

ADAM_LR, ADAM_B1, ADAM_B2, ADAM_EPS, ADAM_WD, ADAM_STEP = 0.001, 0.9, 0.999, 1e-08, 0.01, 10

VMEM_LIMIT = 48 * 1024 * 1024
MESH = pl.DeviceIdType.MESH
N_DEV = 8
N_CHIP = 4


class _Comm:
    def __init__(self, ins, out_shapes, sems, start, finish, aliases=None):
        self.ins, self.out_shapes, self.sems = list(ins), list(out_shapes), list(sems)
        self.start, self.finish = start, finish
        self.aliases = aliases or {}


def _pcall(body, *, name, out_shape, grid=(), in_specs=None, out_specs=None, scratch=(), grid_spec=None, aliases=None,
           comm=None):
    params = pltpu.CompilerParams(vmem_limit_bytes=VMEM_LIMIT)
    kw = dict(name=name, compiler_params=params)
    if comm is None:
        if aliases:
            kw["input_output_aliases"] = aliases
        if grid_spec is not None:
            return pl.pallas_call(body, grid_spec=grid_spec, out_shape=out_shape, **kw)
        return pl.pallas_call(body, grid=grid, in_specs=in_specs, out_specs=out_specs, scratch_shapes=scratch,
                              out_shape=out_shape, **kw)
    single = not isinstance(out_shape, (list, tuple))
    outs = [out_shape] if single else list(out_shape)
    ospecs = [out_specs] if single else list(out_specs)
    n_in, n_out, n_ci, n_co, n_scr = len(in_specs), len(outs), len(comm.ins), len(comm.out_shapes), len(scratch)
    io = dict(aliases or {})
    io.update({n_in + i: n_out + o for i, o in comm.aliases.items()})

    def riding(*refs):
        ins, cins = refs[:n_in], refs[n_in:n_in + n_ci]
        at = n_in + n_ci
        os_, couts = refs[at:at + n_out], refs[at + n_out:at + n_out + n_co]
        at += n_out + n_co
        scr, csems = refs[at:at + n_scr], refs[at + n_scr:]
        if grid:
            first = functools.reduce(jnp.logical_and, [pl.program_id(d) == 0 for d in range(len(grid))])
            last = functools.reduce(jnp.logical_and, [pl.program_id(d) == grid[d] - 1 for d in range(len(grid))])
            pl.when(first)(lambda: comm.start(cins, couts, csems))
            body(*ins, *os_, *scr)
            pl.when(last)(lambda: comm.finish(cins, couts, csems))
        else:
            comm.start(cins, couts, csems)
            body(*ins, *os_, *scr)
            comm.finish(cins, couts, csems)

    call = pl.pallas_call(riding, grid=grid, in_specs=list(in_specs) + [HBM_SPEC] * n_ci, out_specs=ospecs + [HBM_SPEC] * n_co,
                          out_shape=outs + comm.out_shapes, scratch_shapes=list(scratch) + comm.sems,
                          input_output_aliases=io, **kw)

    def run(*args):
        res = call(*args, *comm.ins)
        main = list(res[:n_out])
        return (main[0] if single else main), list(res[n_out:])

    return run


def _pick(dim, target):
    best = None
    for t in range(LANE, min(dim, target) + 1, LANE):
        if dim % t == 0:
            best = t
    return dim if best is None else best


def _sds(shape, dtype):
    return jax.ShapeDtypeStruct(shape, dtype)


def _dw(a, g, *, name, tm=512, tn=1024, comm=None):
    return _mm(a, g, mode="tn", out_dtype=F32, name=name, tm=tm, tn=tn, tk=a.shape[0], n_outer=True, comm=comm)


def _mm(a, b, *, mode, out_dtype, name, tm=1024, tn=1024, tk=4096, n_outer=False, comm=None):
    if mode == "nn":
        (M, K), (K2, N) = a.shape, b.shape
    elif mode == "nt":
        (M, K), (N, K2) = a.shape, b.shape
    else:
        (K, M), (K2, N) = a.shape, b.shape
    assert K == K2, (name, a.shape, b.shape)
    tm, tn, tk = _pick(M, tm), _pick(N, tn), _pick(K, tk)
    nk = K // tk
    if n_outer:
        ij = lambda g0, g1: (g1, g0)
        grid = (N // tn, M // tm, nk)
    else:
        ij = lambda g0, g1: (g0, g1)
        grid = (M // tm, N // tn, nk)
    if mode == "tn":
        a_spec = pl.BlockSpec((tk, tm), lambda g0, g1, k: (k, ij(g0, g1)[0]))
    else:
        a_spec = pl.BlockSpec((tm, tk), lambda g0, g1, k: (ij(g0, g1)[0], k))
    if mode == "nt":
        b_spec = pl.BlockSpec((tn, tk), lambda g0, g1, k: (ij(g0, g1)[1], k))
    else:
        b_spec = pl.BlockSpec((tk, tn), lambda g0, g1, k: (k, ij(g0, g1)[1]))
    o_spec = pl.BlockSpec((tm, tn), lambda g0, g1, k: ij(g0, g1))
    dn = {"nn": (((1,), (0,)), ((), ())), "nt": (((1,), (1,)), ((), ())), "tn": (((0,), (0,)), ((), ()))}[mode]

    def dot(a_ref, b_ref):
        return lax.dot_general(a_ref[...].astype(MX), b_ref[...].astype(MX), dn, preferred_element_type=F32)

    def body_one(a_ref, b_ref, o_ref):
        o_ref[...] = dot(a_ref, b_ref).astype(o_ref.dtype)

    def body_acc(a_ref, b_ref, o_ref, acc_ref):
        k = pl.program_id(2)
        part = dot(a_ref, b_ref)

        @pl.when(k == 0)
        def _():
            acc_ref[...] = part

        @pl.when(k > 0)
        def _():
            acc_ref[...] += part

        @pl.when(k == nk - 1)
        def _():
            o_ref[...] = acc_ref[...].astype(o_ref.dtype)

    return _pcall(body_one if nk == 1 else body_acc, name=name, out_shape=_sds((M, N), out_dtype), grid=grid,
                  in_specs=[a_spec, b_spec], out_specs=o_spec, scratch=[] if nk == 1 else [pltpu.VMEM((tm, tn), F32)],
                  comm=comm)(a, b)


def _gu_shard(q):
    return (q % 2) * 2 + q // 2


def _ffn_up(h, w_st, *, name, tm=512, comm=None):
    T, dm = h.shape
    hw = w_st.shape[2]
    tm = _pick(T, tm)

    def body(h_ref, wg_ref, wu_ref, gu_ref, a_ref):
        hv = h_ref[...]
        g = jnp.dot(hv, wg_ref[0], preferred_element_type=F32)
        u = jnp.dot(hv, wu_ref[0], preferred_element_type=F32)
        gu_ref[:, :hw] = g.astype(gu_ref.dtype)
        gu_ref[:, hw:] = u.astype(gu_ref.dtype)
        a_ref[...] = (g * _sigmoid(g) * u).astype(a_ref.dtype)

    return _pcall(body, name=name, grid=(T // tm, 2),
                  in_specs=[pl.BlockSpec((tm, dm), lambda i, j: (i, 0)), pl.BlockSpec((1, dm, hw), lambda i, j: (j, 0, 0)),
                            pl.BlockSpec((1, dm, hw), lambda i, j: (2 + j, 0, 0))],
                  out_specs=[pl.BlockSpec((tm, 2 * hw), lambda i, j: (i, j)), pl.BlockSpec((tm, hw), lambda i, j: (i, j))],
                  out_shape=[_sds((T, 4 * hw), MX), _sds((T, 2 * hw), MX)], comm=comm)(h, w_st, w_st)


def _ffn_up_first(x, mod3, gain, w_st, *, sub, name, tm=512, comm=None):
    T, dm = x.shape
    hw = w_st.shape[2]
    tm = _pick(T, tm)
    tps = (T // mod3.shape[0]) // tm

    def body(x_ref, mod_ref, n_ref, wg_ref, wu_ref, h_ref, gu_ref, a_ref):
        xv = x_ref[...]
        xn = xv * _rsq(xv) * n_ref[...]
        hv = (xn * (1.0 + mod_ref[0, 3 * sub + 1:3 * sub + 2, :]) + mod_ref[0, 3 * sub:3 * sub + 1, :]).astype(h_ref.dtype)
        h_ref[...] = hv
        g = jnp.dot(hv, wg_ref[0], preferred_element_type=F32)
        u = jnp.dot(hv, wu_ref[0], preferred_element_type=F32)
        gu_ref[:, :hw] = g.astype(gu_ref.dtype)
        gu_ref[:, hw:] = u.astype(gu_ref.dtype)
        a_ref[...] = (g * _sigmoid(g) * u).astype(a_ref.dtype)

    return _pcall(body, name=name, grid=(T // tm,),
                  in_specs=[_row_spec(tm, dm), pl.BlockSpec((1, 9, dm), lambda i: (i // tps, 0, 0)),
                            pl.BlockSpec((1, dm), lambda i: (0, 0)), pl.BlockSpec((1, dm, hw), lambda i: (0, 0, 0)),
                            pl.BlockSpec((1, dm, hw), lambda i: (2, 0, 0))],
                  out_specs=[_row_spec(tm, dm), pl.BlockSpec((tm, 2 * hw), lambda i: (i, 0)), pl.BlockSpec((tm, hw), lambda i: (i, 0))],
                  out_shape=[_sds((T, dm), MX), _sds((T, 4 * hw), MX), _sds((T, 2 * hw), MX)],
                  comm=comm)(x, mod3, gain, w_st, w_st)


def _ffn_up_second(h, w_st, gu, a, *, name, tm=512, comm=None):
    T, dm = h.shape
    hw = w_st.shape[2]
    tm = _pick(T, tm)

    def body(h_ref, wg_ref, wu_ref, gu_in, a_in, gu_ref, a_ref):
        hv = h_ref[...]
        g = jnp.dot(hv, wg_ref[0], preferred_element_type=F32)
        u = jnp.dot(hv, wu_ref[0], preferred_element_type=F32)
        gu_ref[:, :hw] = g.astype(gu_ref.dtype)
        gu_ref[:, hw:] = u.astype(gu_ref.dtype)
        a_ref[...] = (g * _sigmoid(g) * u).astype(a_ref.dtype)

    return _pcall(body, name=name, grid=(T // tm,),
                  in_specs=[_row_spec(tm, dm), pl.BlockSpec((1, dm, hw), lambda i: (1, 0, 0)),
                            pl.BlockSpec((1, dm, hw), lambda i: (3, 0, 0)), HBM_SPEC, HBM_SPEC],
                  out_specs=[pl.BlockSpec((tm, 2 * hw), lambda i: (i, 1)), pl.BlockSpec((tm, hw), lambda i: (i, 1))],
                  out_shape=[_sds(gu.shape, gu.dtype), _sds(a.shape, a.dtype)], aliases={3: 0, 4: 1},
                  comm=comm)(h, w_st, w_st, gu, a)


def _ffn_dact(df, gu, w_out, *, name, tm=512, comm=None):
    T, dm = df.shape
    hw = gu.shape[1] // 4
    tm = _pick(T, tm)

    def body(df_ref, gu_ref, wo_ref, dgu_ref):
        da = lax.dot_general(df_ref[...], wo_ref[...], (((1,), (1,)), ((), ())), preferred_element_type=F32)
        g = gu_ref[:, :hw].astype(F32)
        u = gu_ref[:, hw:].astype(F32)
        s = _sigmoid(g)
        dgu_ref[:, :hw] = (da * u * (s * (1.0 + g * (1.0 - s)))).astype(dgu_ref.dtype)
        dgu_ref[:, hw:] = (da * (g * s)).astype(dgu_ref.dtype)

    return _pcall(body, name=name, grid=(T // tm, 2),
                  in_specs=[pl.BlockSpec((tm, dm), lambda i, j: (i, 0)), pl.BlockSpec((tm, 2 * hw), lambda i, j: (i, j)),
                            pl.BlockSpec((hw, dm), lambda i, j: (j, 0))],
                  out_specs=pl.BlockSpec((tm, 2 * hw), lambda i, j: (i, j)), out_shape=_sds(gu.shape, MX),
                  comm=comm)(df, gu, w_out)


def _ffn_dh(dgu, w_st, *, name, tm=1024, comm=None):
    T = dgu.shape[0]
    _, dm, hw = w_st.shape
    tm = _pick(T, tm)

    def body(d_ref, w_ref, o_ref, acc_ref):
        q = pl.program_id(1)
        part = lax.dot_general(d_ref[...], w_ref[0], (((1,), (1,)), ((), ())), preferred_element_type=F32)

        @pl.when(q == 0)
        def _():
            acc_ref[...] = part

        @pl.when(jnp.logical_and(q > 0, q < 3))
        def _():
            acc_ref[...] += part

        @pl.when(q == 3)
        def _():
            o_ref[...] = acc_ref[...] + part

    return _pcall(body, name=name, grid=(T // tm, 4),
                  in_specs=[pl.BlockSpec((tm, hw), lambda i, q: (i, q)), pl.BlockSpec((1, dm, hw), lambda i, q: (_gu_shard(q), 0, 0))],
                  out_specs=pl.BlockSpec((tm, dm), lambda i, q: (i, 0)), out_shape=_sds((T, dm), F32),
                  scratch=[pltpu.VMEM((tm, dm), F32)], comm=comm)(dgu, w_st)


def _ffn_dw_in(h, dgu, *, name, rows=None, tm=512, comm=None):
    T, dm = h.shape
    hw = dgu.shape[1] // 4
    first, count = rows if rows is not None else (0, dm)
    tm = _pick(count, tm)
    skip = first // tm

    def body(h_ref, d_ref, o_ref):
        o_ref[0] = lax.dot_general(h_ref[...], d_ref[...], (((0,), (0,)), ((), ())), preferred_element_type=F32)

    return _pcall(body, name=name, grid=(4, count // tm),
                  in_specs=[pl.BlockSpec((T, tm), lambda q, i: (0, skip + i)), pl.BlockSpec((T, hw), lambda q, i: (0, q))],
                  out_specs=pl.BlockSpec((1, tm, hw), lambda q, i: (_gu_shard(q), i, 0)),
                  out_shape=_sds((4, count, hw), F32), comm=comm)(h, dgu)


def _rsq(x):
    return lax.rsqrt(jnp.mean(x * x, axis=-1, keepdims=True) + EPS)


def _sigmoid(x):
    return 1.0 / (1.0 + jnp.exp(-x))


def _row_spec(tm, w):
    return pl.BlockSpec((tm, w), lambda i: (i, 0))


def _mm_resid_norm(a, w, x_prev, mod3, gain, *, sub, coef, name, tm=512, comm=None):
    T, k = a.shape
    dm = w.shape[1]
    tm = _pick(T, tm)
    tps = (T // mod3.shape[0]) // tm

    def body(a_ref, w_ref, x_ref, mod_ref, n_ref, f_ref, xo_ref, h_ref):
        f = jnp.dot(a_ref[...], w_ref[...], preferred_element_type=F32)
        f_ref[...] = f
        x = x_ref[...] + coef * mod_ref[0, 3 * sub - 1:3 * sub, :] * f
        xo_ref[...] = x
        xn = x * _rsq(x) * n_ref[...]
        h_ref[...] = (xn * (1.0 + mod_ref[0, 3 * sub + 1:3 * sub + 2, :]) + mod_ref[0, 3 * sub:3 * sub + 1, :]).astype(h_ref.dtype)

    row = _row_spec(tm, dm)
    return _pcall(body, name=name, grid=(T // tm,),
                  in_specs=[_row_spec(tm, k), pl.BlockSpec((k, dm), lambda i: (0, 0)), row,
                            pl.BlockSpec((1, 9, dm), lambda i: (i // tps, 0, 0)), pl.BlockSpec((1, dm), lambda i: (0, 0))],
                  out_specs=[row, row, row], out_shape=[_sds((T, dm), F32), _sds((T, dm), F32), _sds((T, dm), MX)],
                  comm=comm)(a, w, x_prev, mod3, gain)


def _mm_loss(a, w, x2, tgt, mod3, *, name, tm=512):
    T, k = a.shape
    dm = w.shape[1]
    tm = _pick(T, tm)
    tps = (T // mod3.shape[0]) // tm
    mod_spec = pl.BlockSpec((1, 9, dm), lambda i: (i // tps, 0, 0))
    row = _row_spec(tm, dm)
    stat_spec = pl.BlockSpec((1, SUBLANE, dm), lambda i: (i // tps, 0, 0))
    loss_spec = pl.BlockSpec((SUBLANE, LANE), lambda i: (0, 0))

    def body(a_ref, w_ref, x_ref, t_ref, mod_ref, dy_ref, df_ref, st_ref, loss_ref):
        i = pl.program_id(0)
        g = mod_ref[0, 8:9, :]
        f = jnp.dot(a_ref[...], w_ref[...], preferred_element_type=F32)
        err = x_ref[...] + 0.5 * g * f - t_ref[...]
        dy = err * (1.0 / dm)
        dy_ref[...] = dy
        df_ref[...] = (0.5 * g * dy).astype(df_ref.dtype)
        dgate = jnp.sum(0.5 * dy * f, axis=0, keepdims=True)
        part = 0.5 * jnp.sum(jnp.sum(err * err, axis=0, keepdims=True), axis=1, keepdims=True) * (1.0 / dm)

        @pl.when(i % tps == 0)
        def _():
            st_ref[...] = jnp.zeros_like(st_ref)

        @pl.when(i == 0)
        def _():
            loss_ref[...] = jnp.zeros_like(loss_ref)

        st_ref[0, 0:1, :] += dgate
        loss_ref[...] += jnp.broadcast_to(part, loss_ref.shape)

    return _pcall(body, name=name, grid=(T // tm,),
                  in_specs=[_row_spec(tm, k), pl.BlockSpec((k, dm), lambda i: (0, 0)), row, row, mod_spec],
                  out_specs=[row, row, stat_spec, loss_spec],
                  out_shape=[_sds((T, dm), F32), _sds((T, dm), MX), _sds((mod3.shape[0], SUBLANE, dm), F32),
                             _sds((SUBLANE, LANE), F32)])(a, w, x2, tgt, mod3)


def _norm_bwd_tail(dhv, x_ref, dxi_ref, f_ref, mod_ref, n_ref, dx_ref, df_ref, st_ref, *, first, sub, coef):
    x = x_ref[...]
    r = _rsq(x)
    xhat = x * r
    n = n_ref[...]
    d_shift = jnp.sum(dhv, axis=0, keepdims=True)
    d_scale = jnp.sum(dhv * (xhat * n), axis=0, keepdims=True)
    dxn = dhv * (1.0 + mod_ref[0, 3 * sub + 1:3 * sub + 2, :])
    d_gain = jnp.sum(dxn * xhat, axis=0, keepdims=True)
    dxhat = dxn * n
    dx = dxi_ref[...] + r * (dxhat - xhat * jnp.mean(dxhat * xhat, axis=-1, keepdims=True))
    dx_ref[...] = dx

    @pl.when(first)
    def _():
        st_ref[...] = jnp.zeros_like(st_ref)

    st_ref[0, 0:1, :] += d_shift
    st_ref[0, 1:2, :] += d_scale
    st_ref[0, 2:3, :] += d_gain
    if f_ref is not None:
        st_ref[0, 3:4, :] += jnp.sum(coef * dx * f_ref[...], axis=0, keepdims=True)
        df_ref[...] = (coef * mod_ref[0, 3 * sub - 1:3 * sub, :] * dx).astype(df_ref.dtype)


def _ffn_dh_norm(dgu, w_st, x_cur, dx_in, f_prev, mod3, gain, *, sub, coef, name, tm=512, comm=None):
    T = dgu.shape[0]
    _, dm, hw = w_st.shape
    tm = _pick(T, tm)
    nb = mod3.shape[0]
    tps = (T // nb) // tm

    def body(d_ref, w_ref, x_ref, dxi_ref, f_ref, mod_ref, n_ref, dx_ref, df_ref, st_ref, acc_ref):
        i = pl.program_id(0)
        q = pl.program_id(1)
        part = lax.dot_general(d_ref[...], w_ref[0], (((1,), (1,)), ((), ())), preferred_element_type=F32)

        @pl.when(q == 0)
        def _():
            acc_ref[...] = part

        @pl.when(q > 0)
        def _():
            acc_ref[...] += part

        @pl.when(q == 3)
        def _():
            _norm_bwd_tail(acc_ref[...], x_ref, dxi_ref, f_ref, mod_ref, n_ref, dx_ref, df_ref, st_ref,
                           first=i % tps == 0, sub=sub, coef=coef)

    row = pl.BlockSpec((tm, dm), lambda i, q: (i, 0))
    return _pcall(body, name=name, grid=(T // tm, 4),
                  in_specs=[pl.BlockSpec((tm, hw), lambda i, q: (i, q)), pl.BlockSpec((1, dm, hw), lambda i, q: (_gu_shard(q), 0, 0)),
                            row, row, row, pl.BlockSpec((1, 9, dm), lambda i, q: (i // tps, 0, 0)),
                            pl.BlockSpec((1, dm), lambda i, q: (0, 0))],
                  out_specs=[row, row, pl.BlockSpec((1, SUBLANE, dm), lambda i, q: (i // tps, 0, 0))],
                  out_shape=[_sds((T, dm), F32), _sds((T, dm), MX), _sds((nb, SUBLANE, dm), F32)],
                  scratch=[pltpu.VMEM((tm, dm), F32)], comm=comm)(dgu, w_st, x_cur, dx_in, f_prev, mod3, gain)


def _mm_nt_norm(a, b, x_cur, dx_in, f_prev, mod3, gain, *, sub, coef, name, tm=512, comm=None):
    T, k = a.shape
    dm = b.shape[0]
    tm = _pick(T, tm)
    nb = mod3.shape[0]
    tps = (T // nb) // tm

    def body(a_ref, b_ref, x_ref, dxi_ref, f_ref, mod_ref, n_ref, dx_ref, df_ref, st_ref):
        dh = lax.dot_general(a_ref[...], b_ref[...], (((1,), (1,)), ((), ())), preferred_element_type=F32)
        _norm_bwd_tail(dh, x_ref, dxi_ref, f_ref, mod_ref, n_ref, dx_ref, df_ref, st_ref,
                       first=pl.program_id(0) % tps == 0, sub=sub, coef=coef)

    row = _row_spec(tm, dm)
    return _pcall(body, name=name, grid=(T // tm,),
                  in_specs=[_row_spec(tm, k), pl.BlockSpec((dm, k), lambda i: (0, 0)), row, row, row,
                            pl.BlockSpec((1, 9, dm), lambda i: (i // tps, 0, 0)), pl.BlockSpec((1, dm), lambda i: (0, 0))],
                  out_specs=[row, row, pl.BlockSpec((1, SUBLANE, dm), lambda i: (i // tps, 0, 0))],
                  out_shape=[_sds((T, dm), F32), _sds((T, dm), MX), _sds((nb, SUBLANE, dm), F32)],
                  comm=comm)(a, b, x_cur, dx_in, f_prev, mod3, gain)


def _bwd_block(x_cur, dh, dx_in, f_prev, mod3, gain, *, sub, coef, name, tm=256, comm=None):
    T, dm = x_cur.shape
    nb = mod3.shape[0]
    tps = (T // nb) // tm
    has_f = f_prev is not None
    mod_spec = pl.BlockSpec((1, 9, dm), lambda i: (i // tps, 0, 0))
    vec_spec = pl.BlockSpec((1, dm), lambda i: (0, 0))
    stat_spec = pl.BlockSpec((1, SUBLANE, dm), lambda i: (i // tps, 0, 0))
    row = _row_spec(tm, dm)

    def body(*refs):
        if has_f:
            x_ref, dh_ref, dxi_ref, f_ref, mod_ref, n_ref, dx_ref, df_ref, st_ref = refs
        else:
            x_ref, dh_ref, dxi_ref, mod_ref, n_ref, dx_ref, st_ref = refs
            f_ref = df_ref = None
        _norm_bwd_tail(dh_ref[...], x_ref, dxi_ref, f_ref, mod_ref, n_ref, dx_ref, df_ref, st_ref,
                       first=pl.program_id(0) % tps == 0, sub=sub, coef=coef)

    st_shape = _sds((nb, SUBLANE, dm), F32)
    if has_f:
        return _pcall(body, name=name, grid=(T // tm,), in_specs=[row, row, row, row, mod_spec, vec_spec],
                      out_specs=[row, row, stat_spec],
                      out_shape=[_sds((T, dm), F32), _sds((T, dm), MX), st_shape], comm=comm)(x_cur, dh, dx_in, f_prev, mod3, gain)
    return _pcall(body, name=name, grid=(T // tm,), in_specs=[row, row, row, mod_spec, vec_spec],
                  out_specs=[row, stat_spec], out_shape=[_sds((T, dm), F32), st_shape], comm=comm)(x_cur, dh, dx_in, mod3, gain)


def _mix_out_fwd(z, br_pool, attn, w_mla, w_out, x_prev, mod3, gain, *, sub, name, tm=512):
    T = z.shape[0]
    dm = w_out.shape[1]
    tm = _pick(T, tm)
    tps = (T // mod3.shape[0]) // tm
    whole = lambda a: pl.BlockSpec(a.shape, lambda i: (0, 0))

    def body(z_ref, bp_ref, at_ref, wm_ref, wo_ref, x_ref, mod_ref, n_ref, bm_ref, mg_ref, mo_ref, xo_ref, h_ref):
        bm = jnp.dot(at_ref[...], wm_ref[...], preferred_element_type=F32).astype(MX)
        bm_ref[...] = bm
        gp = z_ref[:, Z_GP:Z_GP + D].astype(F32)
        gm = z_ref[:, Z_GM:Z_GM + D].astype(F32)
        merged = (_sigmoid(gp) * bp_ref[...] + _sigmoid(gm) * bm).astype(MX)
        mg_ref[...] = merged
        mo = jnp.dot(merged, wo_ref[...], preferred_element_type=F32)
        mo_ref[...] = mo
        x = x_ref[...] + mod_ref[0, 3 * sub - 1:3 * sub, :] * mo
        xo_ref[...] = x
        xn = x * _rsq(x) * n_ref[...]
        h_ref[...] = (xn * (1.0 + mod_ref[0, 3 * sub + 1:3 * sub + 2, :]) + mod_ref[0, 3 * sub:3 * sub + 1, :]).astype(h_ref.dtype)

    row = _row_spec(tm, dm)
    return _pcall(body, name=name, grid=(T // tm,),
                  in_specs=[_row_spec(tm, Z_W), row, _row_spec(tm, attn.shape[1]), whole(w_mla), whole(w_out), row,
                            pl.BlockSpec((1, 9, dm), lambda i: (i // tps, 0, 0)), pl.BlockSpec((1, dm), lambda i: (0, 0))],
                  out_specs=[row, row, row, row, row],
                  out_shape=[_sds((T, dm), MX), _sds((T, dm), MX), _sds((T, dm), F32), _sds((T, dm), F32), _sds((T, dm), MX)],
                  )(z, br_pool, attn, w_mla, w_out, x_prev, mod3, gain)


def _mix_out_bwd(dmo, w_out, z, br_pool, br_mla, w_mla, *, name, tm=512):
    T = z.shape[0]
    tm = _pick(T, tm)
    whole = lambda a: pl.BlockSpec(a.shape, lambda i: (0, 0))
    nt = (((1,), (1,)), ((), ()))

    def body(dmo_ref, wo_ref, z_ref, bp_ref, bm_ref, wm_ref, dbp_ref, dbm_ref, dg_ref, dat_ref):
        dm = lax.dot_general(dmo_ref[...], wo_ref[...], nt, preferred_element_type=F32)
        sp = _sigmoid(z_ref[:, Z_GP:Z_GP + D].astype(F32))
        sm = _sigmoid(z_ref[:, Z_GM:Z_GM + D].astype(F32))
        dbp_ref[...] = (dm * sp).astype(dbp_ref.dtype)
        dbm = (dm * sm).astype(MX)
        dbm_ref[...] = dbm
        dg_ref[:, :D] = (dm * bp_ref[...].astype(F32) * sp * (1.0 - sp)).astype(dg_ref.dtype)
        dg_ref[:, D:] = (dm * bm_ref[...].astype(F32) * sm * (1.0 - sm)).astype(dg_ref.dtype)
        dat_ref[...] = lax.dot_general(dbm, wm_ref[...], nt, preferred_element_type=F32)

    row = _row_spec(tm, D)
    return _pcall(body, name=name, grid=(T // tm,),
                  in_specs=[row, whole(w_out), _row_spec(tm, Z_W), row, row, whole(w_mla)],
                  out_specs=[row, row, _row_spec(tm, 2 * D), _row_spec(tm, w_mla.shape[0])],
                  out_shape=[_sds((T, D), MX), _sds((T, D), MX), _sds((T, 2 * D), MX), _sds((T, w_mla.shape[0]), F32)],
                  )(dmo, w_out, z, br_pool, br_mla, w_mla)


def _shift_down(x, k, row):
    return jnp.where(row >= k, pltpu.roll(x, k, 0), 0.0)


def _shift_up(x, k, row, n):
    return jnp.where(row < n - k, pltpu.roll(x, n - k, 0), 0.0)


def _pool_fwd(z, pool_grp, pool_scale, *, nb, name):
    T = z.shape[0]
    S = T // nb
    blk = pl.BlockSpec((S, LANE), lambda b, g: (b, g))

    def body(u_ref, w_ref, s_ref, pooled_ref, mixed_ref, scaled_ref):
        g = pl.program_id(1)
        u = u_ref[...].astype(F32)
        row = lax.broadcasted_iota(jnp.int32, u.shape, 0)
        s2 = u + _shift_down(u, 1, row)
        s4 = s2 + _shift_down(s2, 2, row)
        s8 = s4 + _shift_down(s4, 4, row)
        s16 = s8 + _shift_down(s8, 8, row)
        win = jnp.where(g == 0, s2, jnp.where(g == 1, s4, jnp.where(g == 2, s8, s16)))
        width = lax.shift_left(jnp.int32(2), g)
        cnt = jnp.minimum(row + 1, width).astype(F32)
        pooled = (win / cnt - u).astype(MX)
        pooled_ref[...] = pooled
        mixed = jnp.dot(pooled, w_ref[0], preferred_element_type=F32)
        mixed_ref[...] = mixed
        scaled_ref[...] = (mixed * s_ref[...]).astype(scaled_ref.dtype)

    return _pcall(body, name=name, grid=(nb, POOL_G),
                  in_specs=[blk, pl.BlockSpec((1, LANE, LANE), lambda b, g: (g, 0, 0)),
                            pl.BlockSpec((1, LANE), lambda b, g: (0, g))],
                  out_specs=[blk, blk, blk],
                  out_shape=[_sds((T, POOL_W), MX), _sds((T, POOL_W), F32), _sds((T, POOL_W), MX)])(z, pool_grp, pool_scale)


def _pool_bwd(dscaled, mixed, pooled, pool_grp, pool_scale, *, nb, name):
    T = dscaled.shape[0]
    S = T // nb
    blk = pl.BlockSpec((S, LANE), lambda g, b: (b, g))

    def body(ds_ref, mixed_ref, pooled_ref, w_ref, s_ref, du_ref, dw_ref, dsc_ref):
        g = pl.program_id(0)
        b = pl.program_id(1)
        ds = ds_ref[...]
        dsc_ref[0] = jnp.sum(ds * mixed_ref[...], axis=0, keepdims=True)
        dmixed = (ds * s_ref[...]).astype(MX)
        dw = lax.dot_general(pooled_ref[...], dmixed, (((0,), (0,)), ((), ())), preferred_element_type=F32)

        @pl.when(b == 0)
        def _():
            dw_ref[0] = dw

        @pl.when(b > 0)
        def _():
            dw_ref[0] += dw
        dpooled = lax.dot_general(dmixed, w_ref[0], (((1,), (1,)), ((), ())), preferred_element_type=F32)
        row = lax.broadcasted_iota(jnp.int32, dpooled.shape, 0)
        width = lax.shift_left(jnp.int32(2), g)
        q = dpooled / jnp.minimum(row + 1, width).astype(F32)
        r2 = q + _shift_up(q, 1, row, S)
        r4 = r2 + _shift_up(r2, 2, row, S)
        r8 = r4 + _shift_up(r4, 4, row, S)
        r16 = r8 + _shift_up(r8, 8, row, S)
        win = jnp.where(g == 0, r2, jnp.where(g == 1, r4, jnp.where(g == 2, r8, r16)))
        du_ref[...] = (win - dpooled).astype(du_ref.dtype)

    return _pcall(body, name=name, grid=(POOL_G, nb),
                  in_specs=[blk, blk, blk, pl.BlockSpec((1, LANE, LANE), lambda g, b: (g, 0, 0)),
                            pl.BlockSpec((1, LANE), lambda g, b: (0, g))],
                  out_specs=[blk, pl.BlockSpec((1, LANE, LANE), lambda g, b: (g, 0, 0)),
                             pl.BlockSpec((1, 1, LANE), lambda g, b: (b, 0, g))],
                  out_shape=[_sds((T, POOL_W), MX), _sds((POOL_G, LANE, LANE), F32), _sds((nb, 1, POOL_W), F32)],
                  )(dscaled, mixed, pooled, pool_grp, pool_scale)


def _lane_masks(shape):
    lane = lax.broadcasted_iota(jnp.int32, shape, len(shape) - 1)
    m_n = lane < NOPE
    m_r = jnp.logical_and(lane >= KR_LANE, lane < KR_LANE + ROPE)
    first_half = lane < KR_LANE + ROPE // 2
    return m_n, m_r, first_half


def _rot(y, first_half):
    return jnp.where(first_half, -pltpu.roll(y, LANE - ROPE // 2, 1), pltpu.roll(y, ROPE // 2, 1))


def _rot_t(v, first_half, m_r):
    return jnp.where(m_r, jnp.where(first_half, pltpu.roll(v, LANE - ROPE // 2, 1), -pltpu.roll(v, ROPE // 2, 1)), 0.0)


def _mla_in_fwd(z, w_q, w_kv, qa_gain, kva_gain, cos, sin, q_gain, k_gain, *, name, tm=256):
    T = z.shape[0]
    slab = pl.BlockSpec((tm, LANE), lambda i: (i, 0))
    vec = pl.BlockSpec((1, LANE), lambda i: (0, 0))
    whole = lambda a: pl.BlockSpec(a.shape, lambda i: (0, 0))

    def body(z_ref, wq_ref, wkv_ref, qa_ref, kva_ref, cos_ref, sin_ref, qg_ref, kg_ref,
             qn_ref, kvn_ref, qp_ref, kvp_ref, q_ref, k_ref, v_ref):
        ql = z_ref[:, Z_QL:Z_QL + QL].astype(F32)
        kvl = z_ref[:, Z_KV:Z_KV + KVL].astype(F32)
        qn = (ql * _rsq(ql) * qa_ref[...]).astype(MX)
        kvn = (kvl * _rsq(kvl) * kva_ref[...]).astype(MX)
        qn_ref[...] = qn
        kvn_ref[...] = kvn
        qp = jnp.dot(qn, wq_ref[...], preferred_element_type=F32)
        kvp = jnp.dot(kvn, wkv_ref[...], preferred_element_type=F32)
        qp_ref[...] = qp
        kvp_ref[...] = kvp
        m_n, m_r, first_half = _lane_masks((tm, LANE))
        c = cos_ref[...]
        s = sin_ref[...]
        qg = qg_ref[...]
        kg = kg_ref[...]
        xr = z_ref[:, Z_KR:Z_KR + LANE].astype(F32)
        rr = lax.rsqrt(jnp.sum(xr * xr, axis=-1, keepdims=True) * (1.0 / ROPE) + EPS)
        yr = xr * rr * kg
        kr = jnp.where(m_r, yr * c + _rot(yr, first_half) * s, 0.0)
        for h in range(NH):
            x = qp[:, h * LANE:(h + 1) * LANE]
            x2 = x * x
            rn = lax.rsqrt(jnp.sum(jnp.where(m_n, x2, 0.0), axis=-1, keepdims=True) * (1.0 / NOPE) + EPS)
            rq = lax.rsqrt(jnp.sum(jnp.where(m_r, x2, 0.0), axis=-1, keepdims=True) * (1.0 / ROPE) + EPS)
            y = x * jnp.where(m_n, rn, jnp.where(m_r, rq, 0.0)) * qg
            q_ref[:, h * LANE:(h + 1) * LANE] = (y * c + _rot(y, first_half) * s).astype(q_ref.dtype)
            xk = kvp[:, h * LANE:(h + 1) * LANE]
            rk = lax.rsqrt(jnp.sum(jnp.where(m_n, xk * xk, 0.0), axis=-1, keepdims=True) * (1.0 / NOPE) + EPS)
            k_ref[:, h * LANE:(h + 1) * LANE] = (jnp.where(m_n, xk * rk * kg, 0.0) + kr).astype(k_ref.dtype)
        v_ref[...] = kvp[:, NH * LANE:].astype(v_ref.dtype)

    return _pcall(body, name=name, grid=(T // tm,),
                  in_specs=[_row_spec(tm, LAT_W), whole(w_q), whole(w_kv), whole(qa_gain), whole(kva_gain), slab, slab, vec, vec],
                  out_specs=[_row_spec(tm, QL), _row_spec(tm, KVL), _row_spec(tm, NH * LANE), _row_spec(tm, NH * LANE + NH * NOPE),
                             _row_spec(tm, NH * LANE), _row_spec(tm, NH * LANE), _row_spec(tm, NH * NOPE)],
                  out_shape=[_sds((T, QL), MX), _sds((T, KVL), MX), _sds((T, NH * LANE), F32), _sds((T, NH * LANE + NH * NOPE), F32),
                             _sds((T, NH * LANE), MX), _sds((T, NH * LANE), MX), _sds((T, NH * NOPE), MX)],
                  )(z, w_q, w_kv, qa_gain, kva_gain, cos, sin, q_gain, k_gain)


def _mla_in_bwd(dq, dk, dv, qp, kvp, z, w_q, w_kv, qa_gain, kva_gain, cos, sin, q_gain, k_gain, *, nb, name, tm=256):
    T = dq.shape[0]
    tps = (T // nb) // tm
    slab = pl.BlockSpec((tm, LANE), lambda i: (i, 0))
    vec = pl.BlockSpec((1, LANE), lambda i: (0, 0))
    whole = lambda a: pl.BlockSpec(a.shape, lambda i: (0, 0))

    def latent_bwd(x, dy, gain):
        r = _rsq(x)
        xhat = x * r
        dxhat = dy * gain
        return r * (dxhat - xhat * jnp.mean(dxhat * xhat, axis=-1, keepdims=True)), jnp.sum(dy * xhat, axis=0, keepdims=True)

    def body(dq_ref, dk_ref, dv_ref, qp_ref, kvp_ref, z_ref, wq_ref, wkv_ref, qa_ref, kva_ref, cos_ref, sin_ref, qg_ref, kg_ref,
             dqp_ref, dkvp_ref, dkr_ref, dql_ref, dkvl_ref, st_ref, sq_ref, sk_ref):
        i = pl.program_id(0)
        qp, kvp = qp_ref, kvp_ref
        m_n, m_r, first_half = _lane_masks((tm, LANE))
        c = cos_ref[...]
        s = sin_ref[...]
        qg = qg_ref[...]
        kg = kg_ref[...]
        dqg = jnp.zeros((1, LANE), F32)
        dkg = jnp.zeros((1, LANE), F32)
        dkr_sum = jnp.zeros((tm, LANE), F32)
        for h in range(NH):
            x = qp[:, h * LANE:(h + 1) * LANE]
            x2 = x * x
            rn = lax.rsqrt(jnp.sum(jnp.where(m_n, x2, 0.0), axis=-1, keepdims=True) * (1.0 / NOPE) + EPS)
            rq = lax.rsqrt(jnp.sum(jnp.where(m_r, x2, 0.0), axis=-1, keepdims=True) * (1.0 / ROPE) + EPS)
            rfac = jnp.where(m_n, rn, jnp.where(m_r, rq, 0.0))
            xhat = x * rfac
            do = dq_ref[:, h * LANE:(h + 1) * LANE]
            dy = do * c + _rot_t(do * s, first_half, m_r)
            dqg = dqg + jnp.sum(dy * xhat, axis=0, keepdims=True)
            dxhat = dy * qg
            t = dxhat * xhat
            mean_n = jnp.sum(jnp.where(m_n, t, 0.0), axis=-1, keepdims=True) * (1.0 / NOPE)
            mean_r = jnp.sum(jnp.where(m_r, t, 0.0), axis=-1, keepdims=True) * (1.0 / ROPE)
            dqp_ref[:, h * LANE:(h + 1) * LANE] = (
                rfac * (dxhat - xhat * jnp.where(m_n, mean_n, jnp.where(m_r, mean_r, 0.0)))).astype(dqp_ref.dtype)

            xk = kvp[:, h * LANE:(h + 1) * LANE]
            rk = lax.rsqrt(jnp.sum(jnp.where(m_n, xk * xk, 0.0), axis=-1, keepdims=True) * (1.0 / NOPE) + EPS)
            khat = jnp.where(m_n, xk * rk, 0.0)
            dko = dk_ref[:, h * LANE:(h + 1) * LANE]
            dkn = jnp.where(m_n, dko, 0.0)
            dkg = dkg + jnp.sum(dkn * khat, axis=0, keepdims=True)
            dkhat = dkn * kg
            mean_k = jnp.sum(dkhat * khat, axis=-1, keepdims=True) * (1.0 / NOPE)
            dkvp_ref[:, h * LANE:(h + 1) * LANE] = jnp.where(m_n, rk * (dkhat - khat * mean_k), 0.0).astype(dkvp_ref.dtype)
            dkr_sum = dkr_sum + jnp.where(m_r, dko, 0.0)
        dkvp_ref[:, NH * LANE:] = dv_ref[...].astype(dkvp_ref.dtype)

        xr = z_ref[:, Z_KR:Z_KR + LANE].astype(F32)
        rr = lax.rsqrt(jnp.sum(xr * xr, axis=-1, keepdims=True) * (1.0 / ROPE) + EPS)
        rhat = xr * rr
        dyr = dkr_sum * c + _rot_t(dkr_sum * s, first_half, m_r)
        dkg = dkg + jnp.sum(dyr * rhat, axis=0, keepdims=True)
        drhat = dyr * kg
        mean_kr = jnp.sum(drhat * rhat, axis=-1, keepdims=True) * (1.0 / ROPE)
        dkr_ref[...] = jnp.where(m_r, rr * (drhat - rhat * mean_kr), 0.0).astype(dkr_ref.dtype)

        nt = (((1,), (1,)), ((), ()))
        dqn = lax.dot_general(dqp_ref[...], wq_ref[...], nt, preferred_element_type=F32)
        dkvn = lax.dot_general(dkvp_ref[...], wkv_ref[...], nt, preferred_element_type=F32)
        dql, dqa = latent_bwd(z_ref[:, Z_QL:Z_QL + QL].astype(F32), dqn, qa_ref[...])
        dkvl, dkva = latent_bwd(z_ref[:, Z_KV:Z_KV + KVL].astype(F32), dkvn, kva_ref[...])
        dql_ref[...] = dql.astype(dql_ref.dtype)
        dkvl_ref[...] = dkvl.astype(dkvl_ref.dtype)

        @pl.when(i % tps == 0)
        def _():
            st_ref[...] = jnp.zeros_like(st_ref)
            sq_ref[...] = jnp.zeros_like(sq_ref)
            sk_ref[...] = jnp.zeros_like(sk_ref)

        st_ref[0, 0:1, :] += dqg
        st_ref[0, 1:2, :] += dkg
        sq_ref[0, 0:1, :] += dqa
        sk_ref[0, 0:1, :] += dkva

    stat = lambda w: pl.BlockSpec((1, SUBLANE, w), lambda i: (i // tps, 0, 0))
    return _pcall(body, name=name, grid=(T // tm,),
                  in_specs=[_row_spec(tm, NH * LANE), _row_spec(tm, NH * LANE), _row_spec(tm, NH * NOPE),
                            _row_spec(tm, NH * LANE), _row_spec(tm, NH * LANE + NH * NOPE), _row_spec(tm, LAT_W),
                            whole(w_q), whole(w_kv), whole(qa_gain), whole(kva_gain), slab, slab, vec, vec],
                  out_specs=[_row_spec(tm, NH * LANE), _row_spec(tm, NH * LANE + NH * NOPE), slab, _row_spec(tm, QL),
                             _row_spec(tm, KVL), stat(LANE), stat(QL), stat(KVL)],
                  out_shape=[_sds((T, NH * LANE), MX), _sds((T, NH * LANE + NH * NOPE), MX), _sds((T, LANE), MX),
                             _sds((T, QL), MX), _sds((T, KVL), MX), _sds((nb, SUBLANE, LANE), F32),
                             _sds((nb, SUBLANE, QL), F32), _sds((nb, SUBLANE, KVL), F32)],
                  )(dq, dk, dv, qp, kvp, z, w_q, w_kv, qa_gain, kva_gain, cos, sin, q_gain, k_gain)


def _lower_triangle(t):
    return lax.broadcasted_iota(jnp.int32, (t, t), 1) <= lax.broadcasted_iota(jnp.int32, (t, t), 0)


def _attn_fwd(q, k, v, *, nb, name, tq=512, comm=None):
    T = q.shape[0]
    S = T // nb
    tq = _pick(S, tq)
    nq = S // tq
    tk = tq
    npair = NH // 2

    def body(q_ref, k_ref, v_ref, o_ref, lse_ref):
        qi = pl.program_id(2)
        lane = lax.broadcasted_iota(jnp.int32, (tq, LANE), 1)
        qs = [q_ref[:, hh * LANE:(hh + 1) * LANE] for hh in range(2)]

        def block(j, carry, diagonal):
            k0 = pl.multiple_of(j * tk, tk)
            vb = v_ref[pl.ds(k0, tk), :]
            new = []
            for hh in range(2):
                m, l, acc = carry[hh]
                kb = k_ref[pl.ds(k0, tk), hh * LANE:(hh + 1) * LANE]
                s = lax.dot_general(qs[hh], kb, (((1,), (1,)), ((), ())), preferred_element_type=F32) * ATTN_SCALE
                if diagonal:
                    s = jnp.where(_lower_triangle(tq), s, NEG)
                m_new = jnp.maximum(m, jnp.max(s, axis=-1, keepdims=True))
                p = jnp.exp(s - m_new)
                alpha = jnp.exp(m - m_new)
                l = alpha * l + jnp.sum(p, axis=-1, keepdims=True)
                acc = alpha * acc + jnp.dot(p.astype(MX), vb, preferred_element_type=F32)
                new.append((m_new, l, acc))
            return tuple(new)

        init = tuple((jnp.full((tq, 1), NEG, F32), jnp.zeros((tq, 1), F32), jnp.zeros((tq, LANE), F32)) for _ in range(2))
        carry = lax.fori_loop(0, qi, lambda j, c: block(j, c, False), init)
        (m0, l0, acc0), (m1, l1, acc1) = block(qi, carry, True)
        o_ref[...] = jnp.where(lane < NOPE, acc0 / l0, acc1 / l1).astype(o_ref.dtype)
        lse_ref[...] = jnp.where(lane < NOPE, m0 + jnp.log(l0), m1 + jnp.log(l1))

    return _pcall(body, name=name, grid=(nb, npair, nq),
                  in_specs=[pl.BlockSpec((tq, 2 * LANE), lambda b, p, i: (b * nq + i, p)),
                            pl.BlockSpec((S, 2 * LANE), lambda b, p, i: (b, p)),
                            pl.BlockSpec((S, LANE), lambda b, p, i: (b, p))],
                  out_specs=[pl.BlockSpec((tq, LANE), lambda b, p, i: (b * nq + i, p)),
                             pl.BlockSpec((tq, LANE), lambda b, p, i: (b * nq + i, p))],
                  out_shape=[_sds((T, NH * NOPE), MX), _sds((T, NH * NOPE), F32)], comm=comm)(q, k, v)


def _attn_bwd(q, k, v, o, lse, do, *, nb, name, tq=512, comm=None):
    T = q.shape[0]
    S = T // nb
    tq = _pick(S, tq)
    nq = S // tq
    tk = tq
    npair = NH // 2

    def body(q_ref, k_ref, v_ref, o_ref, lse_ref, do_ref, dq_ref, dk_ref, dv_ref, delta_ref):
        lane = lax.broadcasted_iota(jnp.int32, (tq, LANE), 1)
        first = lane < NOPE
        dq_ref[...] = jnp.zeros_like(dq_ref)

        def delta_step(qi, _):
            q0 = pl.multiple_of(qi * tq, tq)
            prod = do_ref[pl.ds(q0, tq), :] * o_ref[pl.ds(q0, tq), :].astype(F32)
            d0 = jnp.sum(jnp.where(first, prod, 0.0), axis=-1, keepdims=True)
            d1 = jnp.sum(jnp.where(first, 0.0, prod), axis=-1, keepdims=True)
            delta_ref[pl.ds(q0, tq), :] = jnp.where(first, d0, d1)
            return 0

        lax.fori_loop(0, nq, delta_step, 0)

        def kv_step(kj, _):
            k0 = pl.multiple_of(kj * tk, tk)
            kbs = [k_ref[pl.ds(k0, tk), hh * LANE:(hh + 1) * LANE] for hh in range(2)]
            vb = v_ref[pl.ds(k0, tk), :]

            def q_block(qi, carry, diagonal):
                dk0, dk1, dv = carry
                dks = [dk0, dk1]
                q0 = pl.multiple_of(qi * tq, tq)
                dov = do_ref[pl.ds(q0, tq), :]
                lse_v = lse_ref[pl.ds(q0, tq), :]
                delta_v = delta_ref[pl.ds(q0, tq), :]
                for hh in range(2):
                    qh = q_ref[pl.ds(q0, tq), hh * LANE:(hh + 1) * LANE]
                    dob = jnp.where(first if hh == 0 else jnp.logical_not(first), dov, 0.0).astype(MX)
                    s = lax.dot_general(qh, kbs[hh], (((1,), (1,)), ((), ())), preferred_element_type=F32) * ATTN_SCALE
                    p = jnp.exp(s - lse_v[:, hh * NOPE:hh * NOPE + 1])
                    if diagonal:
                        p = jnp.where(_lower_triangle(tq), p, 0.0)
                    dp = lax.dot_general(dob, vb, (((1,), (1,)), ((), ())), preferred_element_type=F32)
                    ds = (p * (dp - delta_v[:, hh * NOPE:hh * NOPE + 1]) * ATTN_SCALE).astype(MX)
                    dks[hh] = dks[hh] + lax.dot_general(ds, qh, (((0,), (0,)), ((), ())), preferred_element_type=F32)
                    dv = dv + lax.dot_general(p.astype(MX), dob, (((0,), (0,)), ((), ())), preferred_element_type=F32)
                    dq_ref[pl.ds(q0, tq), hh * LANE:(hh + 1) * LANE] += jnp.dot(ds, kbs[hh], preferred_element_type=F32)
                return dks[0], dks[1], dv

            zero = jnp.zeros((tk, LANE), F32)
            carry = q_block(kj, (zero, zero, zero), True)
            dk0, dk1, dv = lax.fori_loop(kj + 1, nq, lambda qi, c: q_block(qi, c, False), carry)
            dk_ref[pl.ds(k0, tk), 0:LANE] = dk0
            dk_ref[pl.ds(k0, tk), LANE:2 * LANE] = dk1
            dv_ref[pl.ds(k0, tk), :] = dv
            return 0

        lax.fori_loop(0, nq, kv_step, 0)

    pair256 = pl.BlockSpec((S, 2 * LANE), lambda b, p: (b, p))
    pair128 = pl.BlockSpec((S, LANE), lambda b, p: (b, p))
    return _pcall(body, name=name, grid=(nb, npair),
                  in_specs=[pair256, pair256, pair128, pair128, pair128, pair128],
                  out_specs=[pair256, pair256, pair128],
                  out_shape=[_sds((T, NH * LANE), F32), _sds((T, NH * LANE), F32), _sds((T, NH * NOPE), F32)],
                  scratch=[pltpu.VMEM((S, LANE), F32)], comm=comm)(q, k, v, o, lse, do)


def _run(plan, fn, *args, name, **kw):
    rider = plan.rider(name)
    if rider is None:
        return fn(*args, name=name, **kw)
    outs, landed = fn(*args, name=name, comm=rider, **kw)
    plan.landed(name, landed)
    return outs


def _layer_fwd_bwd(x, tgt, mod3, cos, sin, plan, P):
    nb = mod3.shape[0]
    W = plan.W
    h1, gu1, a1 = _run(plan, _ffn_up_first, x, mod3, P["norm_ffn1"], W["ffn1_in"], sub=0, name="ffn1_up_a")
    gu1, a1 = _run(plan, _ffn_up_second, h1, W["ffn1_in"], gu1, a1, name="ffn1_up_b")
    f1, x1, h2 = _run(plan, _mm_resid_norm, a1, W["ffn1_out"], x, mod3, P["norm_mix"], sub=1, coef=0.5, name="ffn1_out")
    z = _run(plan, _mm, h2, W["w_in"], mode="nn", out_dtype=MX, name="mix_in", tn=1664)
    pooled, mixed, scaled = _pool_fwd(z, W["pool_grp"], P["pool_scale"], nb=nb, name="pool_fwd")
    br_pool = _mm(scaled, W["pool_proj"], mode="nn", out_dtype=MX, name="pool_proj")
    qn, kvn, qp, kvp, q, k, v = _mla_in_fwd(z, W["q_up"], W["kv_up"], P["q_a_norm"], P["kv_a_norm"], cos, sin,
                                            P["q_gain"], P["k_gain"], name="mla_in_fwd")
    attn, lse = _run(plan, _attn_fwd, q, k, v, nb=nb, name="attn_fwd")
    br_mla, merged, mo, x2, h3 = _mix_out_fwd(z, br_pool, attn, W["mla_proj"], W["w_out"], x1, mod3, P["norm_ffn2"], sub=2,
                                              name="mix_out", tm=256)
    gu2, a2 = _ffn_up(h3, W["ffn2_in"], name="ffn2_up")
    dy, df2, st_fin, loss = _mm_loss(a2, W["ffn2_out"], x2, tgt, mod3, name="ffn2_out_loss")

    plan.grad("ffn2_out", _dw(a2, df2, name="d_ffn2_out"))
    dgu2 = _ffn_dact(df2, gu2, W["ffn2_out"], name="ffn2_dact")
    plan.grad("ffn2_in", _ffn_dw_in(h3, dgu2, name="d_ffn2_in"))
    dx2, dmo, st3 = _run(plan, _ffn_dh_norm, dgu2, W["ffn2_in"], x2, dy, mo, mod3, P["norm_ffn2"], sub=2, coef=1.0,
                         name="d_ffn2_h")
    plan.grad("w_out", _dw(merged, dmo, name="d_mix_out"))
    dbr_pool, dbr_mla, dgates, dattn = _mix_out_bwd(dmo, W["w_out"], z, br_pool, br_mla, W["mla_proj"], name="mix_out_bwd", tm=256)
    plan.grad("pool_proj", _dw(scaled, dbr_pool, name="d_pool_proj"))
    dscaled = _mm(dbr_pool, W["pool_proj"], mode="nt", out_dtype=F32, name="d_pool_scaled")
    du_pool, d_pool_grp, d_pool_scale = _pool_bwd(dscaled, mixed, pooled, W["pool_grp"], P["pool_scale"], nb=nb, name="pool_bwd")
    plan.grad("mla_proj", _dw(attn, dbr_mla, name="d_mla_proj"))
    dq, dk, dv = _run(plan, _attn_bwd, q, k, v, attn, lse, dattn, nb=nb, name="attn_bwd")
    dqp, dkvp, dkr, dql, dkvl, st_prep, st_q, st_kv = _mla_in_bwd(
        dq, dk, dv, qp, kvp, z, W["q_up"], W["kv_up"], P["q_a_norm"], P["kv_a_norm"], cos, sin, P["q_gain"], P["k_gain"],
        nb=nb, name="mla_in_bwd")
    plan.grad("q_up", _dw(qn, dqp, name="d_q_up"))
    plan.grad("kv_up", _dw(kvn, dkvp, name="d_kv_up"))
    dz = jnp.concatenate([du_pool, dql, dkvl, dkr, dgates], axis=1)
    plan.grad("w_in", _run(plan, _dw, h2, dz, name="d_mix_in", tm=256, tn=1664))
    dx1, df1, st2 = _run(plan, _mm_nt_norm, dz, W["w_in"], x1, dx2, f1, mod3, P["norm_mix"], sub=1, coef=0.5, name="d_mix_h")
    plan.grad("ffn1_out", _run(plan, _dw, a1, df1, name="d_ffn1_out"))
    dgu1 = _run(plan, _ffn_dact, df1, gu1, W["ffn1_out"], name="ffn1_dact")
    half = x.shape[1] // 2
    plan.grad("ffn1_in@0", _run(plan, _ffn_dw_in, h1, dgu1, rows=(0, half), name="d_ffn1_in_a"))
    plan.grad("ffn1_in@1", _run(plan, _ffn_dw_in, h1, dgu1, rows=(half, half), name="d_ffn1_in_b"))
    dh1 = _run(plan, _ffn_dh, dgu1, W["ffn1_in"], name="d_ffn1_h")
    grad_x, st1 = _run(plan, _bwd_block, x, dh1, dx1, None, mod3, P["norm_ffn1"], sub=0, coef=0.0, name="bwd_norm1")

    return loss, grad_x, (st1, st2, st3, st_fin, st_q, st_kv, st_prep, d_pool_scale), d_pool_grp


def _w_in_to_kernel(w):
    k = w.shape[0]
    zeros = lambda n: jnp.zeros((k, n), w.dtype)
    return jnp.concatenate([w[:, 0:1152], zeros(KR_LANE), w[:, 1152:1184], zeros(LANE - KR_LANE - ROPE), w[:, 1184:]], axis=1)


def _w_in_from_kernel(g):
    return jnp.concatenate([g[:, 0:1152], g[:, Z_KR + KR_LANE:Z_KR + KR_LANE + ROPE], g[:, Z_GP:]], axis=1)


def _q_up_to_kernel(w):
    k = w.shape[0]
    return jnp.pad(w.reshape(k, NH, NOPE + ROPE), ((0, 0), (0, 0), (0, LANE - NOPE - ROPE))).reshape(k, NH * LANE)


def _q_up_from_kernel(g):
    k = g.shape[0]
    return g.reshape(k, NH, LANE)[:, :, :NOPE + ROPE].reshape(k, NH * (NOPE + ROPE))


def _kv_up_to_kernel(w):
    k = w.shape[0]
    w3 = w.reshape(k, NH, 2 * NOPE)
    kpart = jnp.pad(w3[:, :, :NOPE], ((0, 0), (0, 0), (0, LANE - NOPE))).reshape(k, NH * LANE)
    return jnp.concatenate([kpart, w3[:, :, NOPE:].reshape(k, NH * NOPE)], axis=1)


def _kv_up_from_kernel(g):
    k = g.shape[0]
    kpart = g[:, :NH * LANE].reshape(k, NH, LANE)[:, :, :NOPE]
    vpart = g[:, NH * LANE:].reshape(k, NH, NOPE)
    return jnp.concatenate([kpart, vpart], axis=2).reshape(k, NH * 2 * NOPE)


def _gain_slab(nope, rope):
    return jnp.concatenate([nope, rope, jnp.zeros((1, LANE - NOPE - ROPE), nope.dtype)], axis=1)


def _rope_tables(positions):
    inv_freq = 10000.0 ** (-jnp.arange(0, ROPE, 2, dtype=F32) / ROPE)
    ang = positions.astype(F32)[:, None] * inv_freq
    ang = jnp.concatenate([ang, ang], axis=-1)
    t = positions.shape[0]
    cos = jnp.concatenate([jnp.ones((t, KR_LANE), F32), jnp.cos(ang), jnp.ones((t, LANE - KR_LANE - ROPE), F32)], axis=1)
    sin = jnp.concatenate([jnp.zeros((t, KR_LANE), F32), jnp.sin(ang), jnp.zeros((t, LANE - KR_LANE - ROPE), F32)], axis=1)
    return cos, sin


def _coords():
    return lax.axis_index("x"), lax.axis_index("y"), lax.axis_index("c")


HBM_SPEC = pl.BlockSpec(memory_space=pl.ANY)
VMEM_SPEC = pl.BlockSpec(memory_space=pltpu.VMEM)


CHIP_RELS = ((1, 0), (0, 1), (1, 1))


def _gather_comm(shards):
    n = len(shards)

    def ici(ins, outs, sems, a, j, x, y, cc):
        half = ins[a].shape[0] // 2
        mine = pl.ds(cc * half, half)
        dx, dy = CHIP_RELS[j]
        return pltpu.make_async_remote_copy(src_ref=ins[a].at[mine], dst_ref=outs[a].at[2 * x + y, mine],
                                            send_sem=sems[0].at[a, j], recv_sem=sems[1].at[a, j],
                                            device_id=(x ^ dx, y ^ dy, cc), device_id_type=MESH)

    def d2d(ins, outs, sems, a, j, x, y, cc, half_of):
        half = ins[a].shape[0] // 2
        dx, dy = CHIP_RELS[j]
        landed = outs[a].at[2 * (x ^ dx) + (y ^ dy), pl.ds(half_of * half, half)]
        return pltpu.make_async_remote_copy(src_ref=landed, dst_ref=landed, send_sem=sems[2].at[a, j], recv_sem=sems[3].at[a, j],
                                            device_id=(x, y, 1 - cc), device_id_type=MESH)

    def own(ins, outs, sems, a, x, y):
        return pltpu.make_async_copy(ins[a], outs[a].at[2 * x + y], sems[4].at[a])

    def start(ins, outs, sems):
        x, y, cc = _coords()
        for a in range(n):
            own(ins, outs, sems, a, x, y).start()
            for j in range(3):
                ici(ins, outs, sems, a, j, x, y, cc).start()

    def finish(ins, outs, sems):
        x, y, cc = _coords()
        for a in range(n):
            for j in range(3):
                ici(ins, outs, sems, a, j, x, y, cc).wait_recv()
                d2d(ins, outs, sems, a, j, x, y, cc, cc).start()
        for a in range(n):
            for j in range(3):
                d2d(ins, outs, sems, a, j, x, y, cc, 1 - cc).wait_recv()
        for a in range(n):
            for j in range(3):
                ici(ins, outs, sems, a, j, x, y, cc).wait_send()
                d2d(ins, outs, sems, a, j, x, y, cc, cc).wait_send()
            own(ins, outs, sems, a, x, y).wait()

    dma = pltpu.SemaphoreType.DMA
    return _Comm(shards, [_sds((N_CHIP,) + s.shape, s.dtype) for s in shards],
                 [dma((n, 3)), dma((n, 3)), dma((n, 3)), dma((n, 3)), dma((n,))], start, finish)


def _swap_comm(parts):
    n = len(parts)

    def copy(ins, outs, sems, a):
        x, y, cc = _coords()
        half = ins[a].shape[1] // 2
        return pltpu.make_async_remote_copy(src_ref=ins[a].at[:, pl.ds((1 - cc) * half, half)], dst_ref=outs[a],
                                            send_sem=sems[0].at[a], recv_sem=sems[1].at[a], device_id=(x, y, 1 - cc),
                                            device_id_type=MESH)

    def start(ins, outs, sems):
        for a in range(n):
            copy(ins, outs, sems, a).start()

    def finish(ins, outs, sems):
        for a in range(n):
            copy(ins, outs, sems, a).wait()

    dma = pltpu.SemaphoreType.DMA
    return _Comm(parts, [_sds((p.shape[0], p.shape[1] // 2, p.shape[2]), p.dtype) for p in parts], [dma((n,)), dma((n,))],
                 start, finish)


def _add_half(full, other, cidx, *, name):
    nch, r, c = full.shape
    half = r // 2
    tr = _pick_rows(half)
    nbk = half // tr
    grid_spec = pltpu.PrefetchScalarGridSpec(
        num_scalar_prefetch=1, grid=(nch, nbk),
        in_specs=[pl.BlockSpec((1, tr, c), lambda j, i, cref: (j, cref[0] * nbk + i, 0)),
                  pl.BlockSpec((1, tr, c), lambda j, i, cref: (j, i, 0))],
        out_specs=pl.BlockSpec((1, tr, c), lambda j, i, cref: (j, i, 0)))

    def body(cref, a_ref, b_ref, o_ref):
        o_ref[...] = (a_ref[...] + b_ref[...]).astype(o_ref.dtype)

    return _pcall(body, name=name, out_shape=_sds((nch, half, c), MX), grid_spec=grid_spec)(cidx, full, other)


def _pick_rows(rows, target=512):
    best = None
    for t in range(16, min(rows, target) + 1, 16):
        if rows % t == 0:
            best = t
    return rows if best is None else best


def _exchange_comm(parts):
    n = len(parts)

    def send(ins, outs, sems, a, j, x, y, cc):
        dx, dy = CHIP_RELS[j]
        return pltpu.make_async_remote_copy(src_ref=ins[a].at[2 * (x ^ dx) + (y ^ dy)], dst_ref=outs[a].at[2 * x + y],
                                            send_sem=sems[0].at[a, j], recv_sem=sems[1].at[a, j],
                                            device_id=(x ^ dx, y ^ dy, cc), device_id_type=MESH)

    def landing(ins, outs, sems, a, j, x, y, cc):
        dx, dy = CHIP_RELS[j]
        peer_chip = 2 * (x ^ dx) + (y ^ dy)
        return pltpu.make_async_remote_copy(src_ref=ins[a].at[peer_chip], dst_ref=outs[a].at[peer_chip], send_sem=sems[0].at[a, j],
                                            recv_sem=sems[1].at[a, j], device_id=(x, y, cc), device_id_type=MESH)

    def own(ins, outs, sems, a, x, y):
        return pltpu.make_async_copy(ins[a].at[2 * x + y], outs[a].at[2 * x + y], sems[2].at[a])

    def start(ins, outs, sems):
        x, y, cc = _coords()
        for a in range(n):
            own(ins, outs, sems, a, x, y).start()
            for j in range(3):
                send(ins, outs, sems, a, j, x, y, cc).start()

    def finish(ins, outs, sems):
        x, y, cc = _coords()
        for a in range(n):
            for j in range(3):
                landing(ins, outs, sems, a, j, x, y, cc).wait_recv()
        for a in range(n):
            for j in range(3):
                send(ins, outs, sems, a, j, x, y, cc).wait_send()
            own(ins, outs, sems, a, x, y).wait()

    dma = pltpu.SemaphoreType.DMA
    return _Comm(parts, [_sds(p.shape, p.dtype) for p in parts], [dma((n, 3)), dma((n, 3)), dma((n,))], start, finish)


def _sum_chips(q, cidx, *, name):
    nch, h, c = q.shape
    tr = _pick_rows(h)
    nbk = h // tr
    grid_spec = pltpu.PrefetchScalarGridSpec(
        num_scalar_prefetch=1, grid=(nbk,),
        in_specs=[pl.BlockSpec((nch, tr, c), lambda i, cref: (0, i, 0))],
        out_specs=pl.BlockSpec((tr, c), lambda i, cref: (cref[0] * nbk + i, 0)))

    def body(cref, q_ref, o_ref):
        acc = q_ref[0].astype(F32) + q_ref[1].astype(F32)
        acc = acc + q_ref[2].astype(F32)
        o_ref[...] = acc + q_ref[3].astype(F32)

    return _pcall(body, name=name, out_shape=_sds((2 * h, c), F32), grid_spec=grid_spec)(cidx, q)


def _join_comm(fulls):
    n = len(fulls)

    def half_copy(outs, sems, a, which):
        x, y, cc = _coords()
        h = outs[a].shape[0] // 2
        rows = outs[a].at[pl.ds((cc if which == 0 else 1 - cc) * h, h)]
        return pltpu.make_async_remote_copy(src_ref=rows, dst_ref=rows, send_sem=sems[0].at[a], recv_sem=sems[1].at[a],
                                            device_id=(x, y, 1 - cc), device_id_type=MESH)

    def start(ins, outs, sems):
        for a in range(n):
            half_copy(outs, sems, a, 0).start()

    def finish(ins, outs, sems):
        for a in range(n):
            half_copy(outs, sems, a, 1).wait_recv()
        for a in range(n):
            half_copy(outs, sems, a, 0).wait_send()

    dma = pltpu.SemaphoreType.DMA
    return _Comm(fulls, [_sds(p.shape, p.dtype) for p in fulls], [dma((n,)), dma((n,))], start, finish,
                 aliases={a: a for a in range(n)})


def _ada_prologue(c, w, b, *, name, comm=None):
    nb, dm = c.shape
    n = w.shape[1]

    def body(c_ref, w_ref, b_ref, call_ref, land_ref, cpad, mod_s, g_send, g_recv, m_send, m_recv):
        x, y, cc = _coords()
        me = 4 * x + 2 * y + cc
        chip = 2 * x + y
        cpad[...] = jnp.zeros_like(cpad)
        cpad[0:nb, :] = c_ref[...]
        copies = []
        for kk in range(1, N_DEV):
            peer = (x ^ (kk >> 2), y ^ ((kk >> 1) & 1), cc ^ (kk & 1))
            cp = pltpu.make_async_remote_copy(src_ref=cpad, dst_ref=call_ref.at[me], send_sem=g_send.at[kk - 1],
                                              recv_sem=g_recv.at[kk - 1], device_id=peer, device_id_type=MESH)
            cp.start()
            copies.append(cp)
        call_ref[me] = cpad[...]
        for kk in range(1, N_DEV):
            pltpu.make_async_remote_copy(src_ref=cpad, dst_ref=call_ref.at[me ^ kk], send_sem=g_send.at[kk - 1],
                                         recv_sem=g_recv.at[kk - 1], device_id=(x, y, cc), device_id_type=MESH).wait_recv()
        cv = call_ref[...].reshape(N_DEV * SUBLANE, dm)
        act = (cv * _sigmoid(cv)).astype(MX)
        mod = jnp.dot(act, w_ref[...].astype(MX), preferred_element_type=F32) + b_ref[...]
        mod_s[...] = mod.reshape(N_DEV, SUBLANE, n)
        for j, (dx, dy) in enumerate(CHIP_RELS):
            cp = pltpu.make_async_remote_copy(src_ref=mod_s.at[4 * (x ^ dx) + 2 * (y ^ dy) + cc], dst_ref=land_ref.at[chip],
                                              send_sem=m_send.at[j], recv_sem=m_recv.at[j],
                                              device_id=(x ^ dx, y ^ dy, cc), device_id_type=MESH)
            cp.start()
            copies.append(cp)
        land_ref[chip] = mod_s[me]
        for j, (dx, dy) in enumerate(CHIP_RELS):
            pltpu.make_async_remote_copy(src_ref=mod_s.at[me], dst_ref=land_ref.at[2 * (x ^ dx) + (y ^ dy)], send_sem=m_send.at[j],
                                         recv_sem=m_recv.at[j], device_id=(x, y, cc), device_id_type=MESH).wait_recv()
        for cp in copies:
            cp.wait_send()

    dma = pltpu.SemaphoreType.DMA
    return _pcall(body, name=name, in_specs=[VMEM_SPEC] * 3, out_specs=[VMEM_SPEC] * 2,
                  out_shape=[_sds((N_DEV, SUBLANE, dm), F32), _sds((N_CHIP, SUBLANE, n), F32)],
                  scratch=[pltpu.VMEM((SUBLANE, dm), F32), pltpu.VMEM((N_DEV, SUBLANE, n), F32),
                           dma((N_DEV - 1,)), dma((N_DEV - 1,)), dma((3,)), dma((3,))], comm=comm)(c, w, b)


def _ada_bwd(c_all, dmod_cols, *, name):
    m, kdim = c_all.shape
    n = dmod_cols.shape[1]
    tn = _pick(n, 1152)

    def body(c_ref, d_ref, o_ref):
        cv = c_ref[...]
        act = (cv * _sigmoid(cv)).astype(MX)
        o_ref[...] = lax.dot_general(act, d_ref[...].astype(MX), (((0,), (0,)), ((), ())), preferred_element_type=F32)

    return _pcall(body, name=name, out_shape=_sds((kdim, n), F32), grid=(n // tn,),
                  in_specs=[pl.BlockSpec((m, kdim), lambda j: (0, 0)), pl.BlockSpec((m, tn), lambda j: (0, j))],
                  out_specs=pl.BlockSpec((kdim, tn), lambda j: (0, j)))(c_all, dmod_cols)


SLAB_W = 1024
RED_ROWS = 8


LOSS_LANE = SLAB_W - LANE


def _pack_stats(st1, st2, st3, st_fin, st_q, st_kv, st_prep, d_pool_scale, loss8, *, name):
    nb = st1.shape[0]
    dm_rows = -(-9 * nb // SUBLANE) * SUBLANE

    def body(s1, s2, s3, sf, sq, skv, sp, sps, loss_ref, o_ref):
        o_ref[...] = jnp.zeros_like(o_ref)
        for s in range(nb):
            rows = [s1[s, 0:1, :], s1[s, 1:2, :], s2[s, 3:4, :], s2[s, 0:1, :], s2[s, 1:2, :], s3[s, 3:4, :], s3[s, 0:1, :],
                    s3[s, 1:2, :], sf[s, 0:1, :]]
            for k, row in enumerate(rows):
                o_ref[9 * s + k:9 * s + k + 1, :] = row

        def over_seq(ref, r):
            acc = ref[0, r:r + 1, :]
            for s in range(1, nb):
                acc = acc + ref[s, r:r + 1, :]
            return acc

        o_ref[dm_rows + 0:dm_rows + 1, :] = over_seq(s1, 2)
        o_ref[dm_rows + 1:dm_rows + 2, :] = over_seq(s2, 2)
        o_ref[dm_rows + 2:dm_rows + 3, :] = over_seq(s3, 2)
        o_ref[dm_rows + 3:dm_rows + 4, 0:POOL_W] = over_seq(sps, 0)
        o_ref[dm_rows + 4:dm_rows + 5, 0:QL] = over_seq(sq, 0)
        o_ref[dm_rows + 5:dm_rows + 6, 0:KVL] = over_seq(skv, 0)
        o_ref[dm_rows + 6:dm_rows + 7, 0:LANE] = over_seq(sp, 0)
        o_ref[dm_rows + 7:dm_rows + 8, 0:LANE] = over_seq(sp, 1)
        o_ref[dm_rows + 7:dm_rows + 8, LOSS_LANE:] = loss_ref[0:1, :]

    return _pcall(body, name=name, out_shape=_sds((dm_rows + RED_ROWS, SLAB_W), F32), in_specs=[VMEM_SPEC] * 9,
                  out_specs=VMEM_SPEC)(st1, st2, st3, st_fin, st_q, st_kv, st_prep, d_pool_scale, loss8)


def _small_allreduce(slab, pool, *, name, comm=None):
    rows, w = slab.shape
    dm = rows - RED_ROWS
    prow, pw = pool.shape
    crow = RED_ROWS + 2 * dm

    def body(slab_ref, pool_ref, red_ref, dmod_ref, ptot_ref, sib_slab, sib_pool, chip_slab, chip_pool, land_slab, land_pool,
             a_send, a_recv, b_send, b_recv):
        x, y, cc = _coords()
        chip = 2 * x + y
        sib = (x, y, 1 - cc)
        to_sib = [pltpu.make_async_remote_copy(src_ref=slab_ref, dst_ref=sib_slab, send_sem=a_send.at[0], recv_sem=a_recv.at[0],
                                               device_id=sib, device_id_type=MESH),
                  pltpu.make_async_remote_copy(src_ref=pool_ref, dst_ref=sib_pool, send_sem=a_send.at[1], recv_sem=a_recv.at[1],
                                               device_id=sib, device_id_type=MESH)]
        for cp in to_sib:
            cp.start()
        for cp in to_sib:
            cp.wait()
        mine_dm, theirs_dm = slab_ref[0:dm, :], sib_slab[0:dm, :]
        chip_slab[0:RED_ROWS, :] = slab_ref[dm:, :] + sib_slab[dm:, :]
        chip_slab[RED_ROWS:RED_ROWS + dm, :] = jnp.where(cc == 0, mine_dm, theirs_dm)
        chip_slab[RED_ROWS + dm:, :] = jnp.where(cc == 0, theirs_dm, mine_dm)
        chip_pool[...] = pool_ref[...] + sib_pool[...]

        sends = []
        for j, (dx, dy) in enumerate(CHIP_RELS):
            peer = (x ^ dx, y ^ dy, cc)
            sends.append(pltpu.make_async_remote_copy(src_ref=chip_slab, dst_ref=land_slab.at[chip], send_sem=b_send.at[j, 0],
                                                      recv_sem=b_recv.at[j, 0], device_id=peer, device_id_type=MESH))
            sends.append(pltpu.make_async_remote_copy(src_ref=chip_pool, dst_ref=land_pool.at[chip], send_sem=b_send.at[j, 1],
                                                      recv_sem=b_recv.at[j, 1], device_id=peer, device_id_type=MESH))
        for cp in sends:
            cp.start()
        land_slab[chip] = chip_slab[...]
        land_pool[chip] = chip_pool[...]
        for j, (dx, dy) in enumerate(CHIP_RELS):
            peer_chip = 2 * (x ^ dx) + (y ^ dy)
            pltpu.make_async_remote_copy(src_ref=chip_slab, dst_ref=land_slab.at[peer_chip], send_sem=b_send.at[j, 0],
                                         recv_sem=b_recv.at[j, 0], device_id=(x, y, cc), device_id_type=MESH).wait_recv()
            pltpu.make_async_remote_copy(src_ref=chip_pool, dst_ref=land_pool.at[peer_chip], send_sem=b_send.at[j, 1],
                                         recv_sem=b_recv.at[j, 1], device_id=(x, y, cc), device_id_type=MESH).wait_recv()
        red = land_slab[0, 0:RED_ROWS, :]
        ptot = land_pool[0]
        for ch in range(1, N_CHIP):
            red = red + land_slab[ch, 0:RED_ROWS, :]
            ptot = ptot + land_pool[ch]
        red_ref[...] = red
        ptot_ref[...] = ptot
        for ch in range(N_CHIP):
            dmod_ref[2 * dm * ch:2 * dm * (ch + 1), :] = land_slab[ch, RED_ROWS:, :]
        for cp in sends:
            cp.wait_send()

    dma = pltpu.SemaphoreType.DMA
    return _pcall(body, name=name, in_specs=[VMEM_SPEC, VMEM_SPEC], out_specs=[VMEM_SPEC] * 3,
                  out_shape=[_sds((RED_ROWS, w), F32), _sds((N_DEV * dm, w), F32), _sds((prow, pw), F32)],
                  scratch=[pltpu.VMEM((rows, w), F32), pltpu.VMEM((prow, pw), F32), pltpu.VMEM((crow, w), F32),
                           pltpu.VMEM((prow, pw), F32), pltpu.VMEM((N_CHIP, crow, w), F32), pltpu.VMEM((N_CHIP, prow, pw), F32),
                           dma((2,)), dma((2,)), dma((3, 2)), dma((3, 2))], comm=comm)(slab, pool)


def _adamw_math(w, g, m, v):
    mn = ADAM_B1 * m + (1.0 - ADAM_B1) * g
    vn = ADAM_B2 * v + (1.0 - ADAM_B2) * (g * g)
    bc1 = 1.0 / (1.0 - ADAM_B1 ** ADAM_STEP)
    bc2 = 1.0 / (1.0 - ADAM_B2 ** ADAM_STEP)
    return -ADAM_LR * ((mn * bc1) / (jnp.sqrt(vn * bc2) + ADAM_EPS) + ADAM_WD * w), mn, vn


def _small_update(red, dmod_all, pool_total, nb, params, *, name):
    names = list(SMALL)
    dm = dmod_all.shape[0] // N_DEV

    def grad_of(nm, red_ref, dmod_ref, ptot_ref):
        if nm == "b_ada":
            acc = None
            for d in range(N_DEV):
                for s in range(nb):
                    blk = dmod_ref[d * dm + 9 * s:d * dm + 9 * s + 9, :]
                    acc = blk if acc is None else acc + blk
            return jnp.concatenate([acc[k:k + 1, :] for k in range(9)], axis=1)
        if nm == "pool_grp":
            return ptot_ref[...]
        row, lo, n = {"norm_ffn1": (0, 0, D), "norm_mix": (1, 0, D), "norm_ffn2": (2, 0, D), "pool_scale": (3, 0, POOL_W),
                      "q_a_norm": (4, 0, QL), "kv_a_norm": (5, 0, KVL), "q_norm_nope": (6, 0, NOPE),
                      "q_norm_rope": (6, NOPE, ROPE), "k_norm_nope": (7, 0, NOPE), "k_norm_rope": (7, KR_LANE, ROPE)}[nm]
        return red_ref[row:row + 1, lo:lo + n]

    def body(*refs):
        red_ref, dmod_ref, ptot_ref = refs[:3]
        ins = refs[3:3 + 3 * len(names)]
        outs = refs[3 + 3 * len(names):]
        outs[4 * len(names)][...] = red_ref[RED_ROWS - 1:RED_ROWS, LOSS_LANE:]
        for i, nm in enumerate(names):
            g = grad_of(nm, red_ref, dmod_ref, ptot_ref)
            d, mn, vn = _adamw_math(ins[3 * i][...], g, ins[3 * i + 1][...], ins[3 * i + 2][...])
            outs[4 * i][...] = g
            outs[4 * i + 1][...] = d
            outs[4 * i + 2][...] = mn
            outs[4 * i + 3][...] = vn

    flat_in = [a for nm in names for a in params[nm]]
    out_shape = [_sds(params[nm][0].shape, F32) for nm in names for _ in range(4)] + [_sds((1, LANE), F32)]
    res = _pcall(body, name=name, in_specs=[VMEM_SPEC] * (3 + len(flat_in)), out_specs=[VMEM_SPEC] * len(out_shape),
                 out_shape=out_shape)(red, dmod_all, pool_total, *flat_in)
    return {nm: tuple(res[4 * i:4 * i + 4]) for i, nm in enumerate(names)}, res[-1]


def _adamw(w, g, m, v, *, name, comm=None):
    r, c = w.shape
    tr = _pick_rows(r, 256)
    tc = c if tr < r else _pick(c, 256)
    spec = pl.BlockSpec((tr, tc), lambda i, j: (i, j))
    bc1 = 1.0 / (1.0 - ADAM_B1 ** ADAM_STEP)
    bc2 = 1.0 / (1.0 - ADAM_B2 ** ADAM_STEP)

    def body(w_ref, g_ref, m_ref, v_ref, d_ref, mo_ref, vo_ref):
        gv = g_ref[...]
        mn = ADAM_B1 * m_ref[...] + (1.0 - ADAM_B1) * gv
        vn = ADAM_B2 * v_ref[...] + (1.0 - ADAM_B2) * (gv * gv)
        mo_ref[...] = mn
        vo_ref[...] = vn
        d_ref[...] = -ADAM_LR * ((mn * bc1) / (jnp.sqrt(vn * bc2) + ADAM_EPS) + ADAM_WD * w_ref[...])

    out = _sds((r, c), F32)
    return _pcall(body, name=name, out_shape=[out, out, out], grid=(r // tr, c // tc), in_specs=[spec] * 4, out_specs=[spec] * 3,
                  comm=comm)(w, g, m, v)


BIG = ("w_ffn1_in", "w_ffn1_out", "w_in", "w_pool_proj", "w_q_up", "w_kv_up", "w_mla_proj", "w_out", "w_ffn2_in", "w_ffn2_out")
ROW_SHARDED = ("w_ffn1_out", "w_out", "w_ffn2_out")
KERNEL_NAME = {"w_ffn1_in": "ffn1_in", "w_ffn1_out": "ffn1_out", "w_in": "w_in", "w_pool_proj": "pool_proj", "w_q_up": "q_up",
               "w_kv_up": "kv_up", "w_mla_proj": "mla_proj", "w_out": "w_out", "w_ffn2_in": "ffn2_in", "w_ffn2_out": "ffn2_out"}
WEIGHTS = ("w_ada", "b_ada", "norm_ffn1", "w_ffn1_in", "w_ffn1_out", "norm_mix", "w_in", "pool_grp", "pool_scale", "w_pool_proj",
           "q_a_norm", "w_q_up", "kv_a_norm", "w_kv_up", "q_norm_nope", "q_norm_rope", "k_norm_nope", "k_norm_rope", "w_mla_proj",
           "w_out", "norm_ffn2", "w_ffn2_in", "w_ffn2_out")
SMALL = ("b_ada", "norm_ffn1", "norm_mix", "pool_grp", "pool_scale", "q_a_norm", "kv_a_norm", "q_norm_nope", "q_norm_rope",
         "k_norm_nope", "k_norm_rope", "norm_ffn2")


def _assemble(name, stacked):
    if name in ROW_SHARDED:
        return stacked.reshape(stacked.shape[0] * stacked.shape[1], stacked.shape[2])
    return jnp.transpose(stacked, (1, 0, 2)).reshape(stacked.shape[1], stacked.shape[0] * stacked.shape[2])


def _split(name, full):
    if name in ROW_SHARDED:
        return full.reshape(N_CHIP, full.shape[0] // N_CHIP, full.shape[1])
    return jnp.transpose(full.reshape(full.shape[0], N_CHIP, full.shape[1] // N_CHIP), (1, 0, 2))


GU = ("w_ffn1_in", "w_ffn2_in")
NARROW = ("w_in", "w_q_up")


MIX_SMALL = ("w_out", "w_pool_proj", "w_mla_proj", "w_q_up", "w_kv_up")
RIDES = {
    "ada_prologue": (("gather", ("w_ffn1_in",)),),
    "ffn1_up_a": (("gather", ("w_ffn1_out",)),),
    "ffn1_up_b": (("gather", ("w_in",)),),
    "mix_in": (("gather", MIX_SMALL),),
    "attn_fwd": (("gather", ("w_ffn2_in", "w_ffn2_out")),),
    "d_ffn2_h": (("swap", ("w_ffn2_out", "w_ffn2_in")),),
    "attn_bwd": (("exchange", ("w_ffn2_out", "w_ffn2_in")),),
    "d_mix_in": (("swap", MIX_SMALL),),
    "d_mix_h": (("exchange", MIX_SMALL), ("swap", ("w_in",))),
    "ffn1_dact": (("exchange", ("w_in",)), ("swap", ("w_ffn1_out",))),
    "d_ffn1_in_a": (("exchange", ("w_ffn1_out",)),),
    "d_ffn1_in_b": (("swap", ("w_ffn1_in@0",)),),
    "d_ffn1_h": (("exchange", ("w_ffn1_in@0",)), ("swap", ("w_ffn1_in@1",))),
    "bwd_norm1": (("exchange", ("w_ffn1_in@1",)),),
    "small_allreduce": (("join", tuple(n for n in BIG if n != "w_ffn1_in") + ("w_ffn1_in@0", "w_ffn1_in@1")),),
}


def _kname(n):
    base, _, part = n.partition("@")
    return KERNEL_NAME[base] + ("@" + part if part else "")


def _both(comms):
    if len(comms) == 1:
        return comms[0]
    ins, outs, sems, aliases, spans = [], [], [], {}, []
    for c in comms:
        spans.append((len(ins), len(c.ins), len(outs), len(c.out_shapes), len(sems), len(c.sems)))
        aliases.update({len(ins) + i: len(outs) + o for i, o in c.aliases.items()})
        ins, outs, sems = ins + c.ins, outs + c.out_shapes, sems + c.sems

    def each(which):
        def run(i_, o_, s_):
            for c, (ia, ni, oa, no, sa, ns) in zip(comms, spans):
                getattr(c, which)(i_[ia:ia + ni], o_[oa:oa + no], s_[sa:sa + ns])
        return run

    return _Comm(ins, outs, sems, each("start"), each("finish"), aliases)


class _ExchangePlan:
    def __init__(self, shards, cidx):
        self.shards, self.cidx = shards, cidx
        self.W, self.G, self.parts, self.pre, self.reduced, self.joined = {}, {}, {}, {}, {}, {}

    def grad(self, key, g):
        self.G[key] = g

    def rider(self, name):
        comms = []
        for kind, names in RIDES.get(name, ()):
            if kind == "gather":
                comms.append(_gather_comm([self.shards[n] for n in names]))
            elif kind == "swap":
                for n in names:
                    self.parts[n] = self._stacked(n)
                comms.append(_swap_comm([self.parts[n] for n in names]))
            elif kind == "exchange":
                comms.append(_exchange_comm([self.pre[n] for n in names]))
            else:
                comms.append(_join_comm([self.reduced[n] for n in names]))
        return _both(comms) if comms else None

    def landed(self, name, outs):
        at = 0
        for kind, names in RIDES[name]:
            for n, o in zip(names, outs[at:at + len(names)]):
                if kind == "gather":
                    self.W[KERNEL_NAME[n]] = self._to_kernel(n, o)
                elif kind == "swap":
                    self.pre[n] = _add_half(self.parts[n], o, self.cidx, name="rs_add_" + _kname(n))
                elif kind == "exchange":
                    self.reduced[n] = _sum_chips(o, self.cidx, name="rs_sum_" + _kname(n))
                else:
                    self.joined[n] = o
            at += len(names)

    @staticmethod
    def _to_kernel(n, stacked):
        if n in GU:
            return stacked
        full = _assemble(n, stacked)
        return {"w_in": _w_in_to_kernel, "w_q_up": _q_up_to_kernel, "w_kv_up": _kv_up_to_kernel}.get(n, lambda w: w)(full)

    def _stacked(self, n):
        g = self.G[_kname(n)]
        if n.partition("@")[0] in GU:
            return g
        full = {"w_in": _w_in_from_kernel, "w_q_up": _q_up_from_kernel, "w_kv_up": _kv_up_from_kernel}.get(n, lambda w: w)(g)
        return _split(n, full)


def kernel(x, c, positions, w_ada, b_ada, norm_ffn1, w_ffn1_in, w_ffn1_out, norm_mix, w_in, pool_grp, pool_scale, w_pool_proj, q_a_norm, w_q_up, kv_a_norm, w_kv_up, q_norm_nope, q_norm_rope, k_norm_nope, k_norm_rope, w_mla_proj, w_out, norm_ffn2, w_ffn2_in, w_ffn2_out, loss_target, m_w_ada, m_b_ada, m_norm_ffn1, m_w_ffn1_in, m_w_ffn1_out, m_norm_mix, m_w_in, m_pool_grp, m_pool_scale, m_w_pool_proj, m_q_a_norm, m_w_q_up, m_kv_a_norm, m_w_kv_up, m_q_norm_nope, m_q_norm_rope, m_k_norm_nope, m_k_norm_rope, m_w_mla_proj, m_w_out, m_norm_ffn2, m_w_ffn2_in, m_w_ffn2_out, v_w_ada, v_b_ada, v_norm_ffn1, v_w_ffn1_in, v_w_ffn1_out, v_norm_mix, v_w_in, v_pool_grp, v_pool_scale, v_w_pool_proj, v_q_a_norm, v_w_q_up, v_kv_a_norm, v_w_kv_up, v_q_norm_nope, v_q_norm_rope, v_k_norm_nope, v_k_norm_rope, v_w_mla_proj, v_w_out, v_norm_ffn2, v_w_ffn2_in, v_w_ffn2_out):
    args = dict(locals())
    wts = {n: args[n][0] for n in WEIGHTS}
    mom = {n: args["m_" + n][0] for n in WEIGHTS}
    var = {n: args["v_" + n][0] for n in WEIGHTS}
    nb, seq, dm = x.shape
    tokens = nb * seq
    xi, yi, ci = _coords()
    chip = 2 * xi + yi

    cidx = ci.astype(jnp.int32).reshape(1)
    plan = _ExchangePlan({n: wts[n].astype(MX) for n in BIG}, cidx)
    plan.W["pool_grp"] = wts["pool_grp"].astype(MX)

    ncol = w_ada.shape[2]
    b_cols = lax.dynamic_slice_in_dim(wts["b_ada"].reshape(1, -1), chip * ncol, ncol, axis=1)
    c_slots, mod_slots = _run(plan, _ada_prologue, c, wts["w_ada"], b_cols, name="ada_prologue")
    c_all = c_slots[:, :nb].reshape(N_DEV * nb, dm)
    mod3 = jnp.transpose(mod_slots[:, :nb], (1, 0, 2)).reshape(nb, 9, dm)
    P = {"norm_ffn1": wts["norm_ffn1"].reshape(1, dm), "norm_mix": wts["norm_mix"].reshape(1, dm),
         "norm_ffn2": wts["norm_ffn2"].reshape(1, dm), "pool_scale": wts["pool_scale"].reshape(1, POOL_W),
         "q_a_norm": wts["q_a_norm"].reshape(1, QL), "kv_a_norm": wts["kv_a_norm"].reshape(1, KVL),
         "q_gain": _gain_slab(wts["q_norm_nope"].reshape(1, NOPE), wts["q_norm_rope"].reshape(1, ROPE)),
         "k_gain": _gain_slab(wts["k_norm_nope"].reshape(1, NOPE), wts["k_norm_rope"].reshape(1, ROPE))}
    cos, sin = _rope_tables(positions.reshape(tokens))

    loss8, grad_x, stats, d_pool_grp = _layer_fwd_bwd(x.reshape(tokens, dm), loss_target.reshape(tokens, dm), mod3, cos, sin, plan, P)

    slab = _pack_stats(*stats, loss8, name="pack_stats")
    red, dmod_rows, pool_total = _run(plan, _small_allreduce, slab, d_pool_grp.reshape(POOL_G * LANE, LANE), name="small_allreduce")
    grads = {n: plan.joined[n] for n in BIG if n != "w_ffn1_in"}
    grads["w_ffn1_in"] = jnp.concatenate([plan.joined["w_ffn1_in@0"], plan.joined["w_ffn1_in@1"]], axis=0)
    dm_rows = dmod_rows.shape[0] // N_DEV
    dmod_all = dmod_rows.reshape(N_DEV, dm_rows, SLAB_W)[:, :9 * nb].reshape(N_DEV * nb, 9 * dm)
    dmod_cols = lax.dynamic_slice_in_dim(dmod_all, chip * ncol, ncol, axis=1)
    grads["w_ada"] = _ada_bwd(c_all, dmod_cols, name="ada_bwd")

    delta, new_m, new_v = {}, {}, {}
    as2d = lambda a: a.reshape(POOL_G * LANE, LANE) if a.ndim == 4 else a.reshape(1, -1)
    upd, loss_row = _small_update(red, dmod_rows, pool_total, nb,
                                  {n: tuple(as2d(args[p + n]) for p in ("", "m_", "v_")) for n in SMALL}, name="small_update")
    loss = loss_row[0, 0]
    for n in SMALL:
        grads[n], delta[n], new_m[n], new_v[n] = upd[n]
    for n in ("w_ada",) + BIG:
        if n in NARROW:
            res = _adamw(wts[n].T, grads[n].T, mom[n].T, var[n].T, name="adamw_" + n)
            delta[n], new_m[n], new_v[n] = (r.T for r in res)
        else:
            delta[n], new_m[n], new_v[n] = _adamw(wts[n], grads[n], mom[n], var[n], name="adamw_" + n)

    def lead(a, n):
        return a.reshape((1,) + wts[n].shape)

    return (loss, grad_x.reshape(nb, seq, dm), *[lead(grads[n], n) for n in WEIGHTS], *[lead(delta[n], n) for n in WEIGHTS],
            *[lead(new_m[n], n) for n in WEIGHTS], *[lead(new_v[n], n) for n in WEIGHTS])
```

```python
import functools
import math

import jax
import jax.numpy as jnp
from jax import lax
from jax.experimental import pallas as pl
from jax.experimental.pallas import tpu as pltpu

F32 = jnp.float32
MX = jnp.bfloat16

D = 1024
DFF = 2816
NH = 8
POOL_W = 512
POOL_G = 4
QL = 384
KVL = 256
ROPE = 32
NOPE = 64
LANE = 128
SUBLANE = 8
EPS = 1e-6
ATTN_SCALE = 1.0 / math.sqrt(96.0)
NEG = -1e30

Z_UP, Z_QL, Z_KV, Z_KR, Z_GP, Z_GM, Z_W = 0, 512, 896, 1152, 1280, 2304, 3328
KR_LANE = 64
LAT_W = 1280

ADAM_LR, ADAM_B1, ADAM_B2, ADAM_EPS, ADAM_WD, ADAM_STEP = 0.001, 0.9, 0.999, 1e-08, 0.01, 10

VMEM_LIMIT = 48 * 1024 * 1024
MESH = pl.DeviceIdType.MESH
N_DEV = 8
N_CHIP = 4


class _Comm:
    def __init__(self, ins, out_shapes, sems, start, finish, aliases=None):
        self.ins, self.out_shapes, self.sems = list(ins), list(out_shapes), list(sems)
        self.start, self.finish = start, finish
        self.aliases = aliases or {}


def _pcall(body, *, name, out_shape, grid=(), in_specs=None, out_specs=None, scratch=(), grid_spec=None, aliases=None,
           comm=None):
    params = pltpu.CompilerParams(vmem_limit_bytes=VMEM_LIMIT)
    kw = dict(name=name, compiler_params=params)
    if comm is None:
        if aliases:
            kw["input_output_aliases"] = aliases
        if grid_spec is not None:
            return pl.pallas_call(body, grid_spec=grid_spec, out_shape=out_shape, **kw)
        return pl.pallas_call(body, grid=grid, in_specs=in_specs, out_specs=out_specs, scratch_shapes=scratch,
                              out_shape=out_shape, **kw)
    single = not isinstance(out_shape, (list, tuple))
    outs = [out_shape] if single else list(out_shape)
    ospecs = [out_specs] if single else list(out_specs)
    n_in, n_out, n_ci, n_co, n_scr = len(in_specs), len(outs), len(comm.ins), len(comm.out_shapes), len(scratch)
    io = dict(aliases or {})
    io.update({n_in + i: n_out + o for i, o in comm.aliases.items()})

    def riding(*refs):
        ins, cins = refs[:n_in], refs[n_in:n_in + n_ci]
        at = n_in + n_ci
        os_, couts = refs[at:at + n_out], refs[at + n_out:at + n_out + n_co]
        at += n_out + n_co
        scr, csems = refs[at:at + n_scr], refs[at + n_scr:]
        if grid:
            first = functools.reduce(jnp.logical_and, [pl.program_id(d) == 0 for d in range(len(grid))])
            last = functools.reduce(jnp.logical_and, [pl.program_id(d) == grid[d] - 1 for d in range(len(grid))])
            pl.when(first)(lambda: comm.start(cins, couts, csems))
            body(*ins, *os_, *scr)
            pl.when(last)(lambda: comm.finish(cins, couts, csems))
        else:
            comm.start(cins, couts, csems)
            body(*ins, *os_, *scr)
            comm.finish(cins, couts, csems)

    call = pl.pallas_call(riding, grid=grid, in_specs=list(in_specs) + [HBM_SPEC] * n_ci, out_specs=ospecs + [HBM_SPEC] * n_co,
                          out_shape=outs + comm.out_shapes, scratch_shapes=list(scratch) + comm.sems,
                          input_output_aliases=io, **kw)

    def run(*args):
        res = call(*args, *comm.ins)
        main = list(res[:n_out])
        return (main[0] if single else main), list(res[n_out:])

    return run


def _pick(dim, target):
    best = None
    for t in range(LANE, min(dim, target) + 1, LANE):
        if dim % t == 0:
            best = t
    return dim if best is None else best


def _sds(shape, dtype):
    return jax.ShapeDtypeStruct(shape, dtype)


def _dw(a, g, *, name, tm=512, tn=1024, comm=None):
    return _mm(a, g, mode="tn", out_dtype=F32, name=name, tm=tm, tn=tn, tk=a.shape[0], n_outer=True, comm=comm)


def _mm(a, b, *, mode, out_dtype, name, tm=1024, tn=1024, tk=4096, n_outer=False, comm=None):
    if mode == "nn":
        (M, K), (K2, N) = a.shape, b.shape
    elif mode == "nt":
        (M, K), (N, K2) = a.shape, b.shape
    else:
        (K, M), (K2, N) = a.shape, b.shape
    assert K == K2, (name, a.shape, b.shape)
    tm, tn, tk = _pick(M, tm), _pick(N, tn), _pick(K, tk)
    nk = K // tk
    if n_outer:
        ij = lambda g0, g1: (g1, g0)
        grid = (N // tn, M // tm, nk)
    else:
        ij = lambda g0, g1: (g0, g1)
        grid = (M // tm, N // tn, nk)
    if mode == "tn":
        a_spec = pl.BlockSpec((tk, tm), lambda g0, g1, k: (k, ij(g0, g1)[0]))
    else:
        a_spec = pl.BlockSpec((tm, tk), lambda g0, g1, k: (ij(g0, g1)[0], k))
    if mode == "nt":
        b_spec = pl.BlockSpec((tn, tk), lambda g0, g1, k: (ij(g0, g1)[1], k))
    else:
        b_spec = pl.BlockSpec((tk, tn), lambda g0, g1, k: (k, ij(g0, g1)[1]))
    o_spec = pl.BlockSpec((tm, tn), lambda g0, g1, k: ij(g0, g1))
    dn = {"nn": (((1,), (0,)), ((), ())), "nt": (((1,), (1,)), ((), ())), "tn": (((0,), (0,)), ((), ()))}[mode]

    def dot(a_ref, b_ref):
        return lax.dot_general(a_ref[...].astype(MX), b_ref[...].astype(MX), dn, preferred_element_type=F32)

    def body_one(a_ref, b_ref, o_ref):
        o_ref[...] = dot(a_ref, b_ref).astype(o_ref.dtype)

    def body_acc(a_ref, b_ref, o_ref, acc_ref):
        k = pl.program_id(2)
        part = dot(a_ref, b_ref)

        @pl.when(k == 0)
        def _():
            acc_ref[...] = part

        @pl.when(k > 0)
        def _():
            acc_ref[...] += part

        @pl.when(k == nk - 1)
        def _():
            o_ref[...] = acc_ref[...].astype(o_ref.dtype)

    return _pcall(body_one if nk == 1 else body_acc, name=name, out_shape=_sds((M, N), out_dtype), grid=grid,
                  in_specs=[a_spec, b_spec], out_specs=o_spec, scratch=[] if nk == 1 else [pltpu.VMEM((tm, tn), F32)],
                  comm=comm)(a, b)


def _gu_shard(q):
    return (q % 2) * 2 + q // 2


def _ffn_up(h, w_st, *, name, tm=512, comm=None):
    T, dm = h.shape
    hw = w_st.shape[2]
    tm = _pick(T, tm)

    def body(h_ref, wg_ref, wu_ref, gu_ref, a_ref):
        hv = h_ref[...]
        g = jnp.dot(hv, wg_ref[0], preferred_element_type=F32)
        u = jnp.dot(hv, wu_ref[0], preferred_element_type=F32)
        gu_ref[:, :hw] = g.astype(gu_ref.dtype)
        gu_ref[:, hw:] = u.astype(gu_ref.dtype)
        a_ref[...] = (g * _sigmoid(g) * u).astype(a_ref.dtype)

    return _pcall(body, name=name, grid=(T // tm, 2),
                  in_specs=[pl.BlockSpec((tm, dm), lambda i, j: (i, 0)), pl.BlockSpec((1, dm, hw), lambda i, j: (j, 0, 0)),
                            pl.BlockSpec((1, dm, hw), lambda i, j: (2 + j, 0, 0))],
                  out_specs=[pl.BlockSpec((tm, 2 * hw), lambda i, j: (i, j)), pl.BlockSpec((tm, hw), lambda i, j: (i, j))],
                  out_shape=[_sds((T, 4 * hw), MX), _sds((T, 2 * hw), MX)], comm=comm)(h, w_st, w_st)


def _ffn_up_first(x, mod3, gain, w_st, *, sub, name, tm=512, comm=None):
    T, dm = x.shape
    hw = w_st.shape[2]
    tm = _pick(T, tm)
    tps = (T // mod3.shape[0]) // tm

    def body(x_ref, mod_ref, n_ref, wg_ref, wu_ref, h_ref, gu_ref, a_ref):
        xv = x_ref[...]
        xn = xv * _rsq(xv) * n_ref[...]
        hv = (xn * (1.0 + mod_ref[0, 3 * sub + 1:3 * sub + 2, :]) + mod_ref[0, 3 * sub:3 * sub + 1, :]).astype(h_ref.dtype)
        h_ref[...] = hv
        g = jnp.dot(hv, wg_ref[0], preferred_element_type=F32)
        u = jnp.dot(hv, wu_ref[0], preferred_element_type=F32)
        gu_ref[:, :hw] = g.astype(gu_ref.dtype)
        gu_ref[:, hw:] = u.astype(gu_ref.dtype)
        a_ref[...] = (g * _sigmoid(g) * u).astype(a_ref.dtype)

    return _pcall(body, name=name, grid=(T // tm,),
                  in_specs=[_row_spec(tm, dm), pl.BlockSpec((1, 9, dm), lambda i: (i // tps, 0, 0)),
                            pl.BlockSpec((1, dm), lambda i: (0, 0)), pl.BlockSpec((1, dm, hw), lambda i: (0, 0, 0)),
                            pl.BlockSpec((1, dm, hw), lambda i: (2, 0, 0))],
                  out_specs=[_row_spec(tm, dm), pl.BlockSpec((tm, 2 * hw), lambda i: (i, 0)), pl.BlockSpec((tm, hw), lambda i: (i, 0))],
                  out_shape=[_sds((T, dm), MX), _sds((T, 4 * hw), MX), _sds((T, 2 * hw), MX)],
                  comm=comm)(x, mod3, gain, w_st, w_st)


def _ffn_up_second(h, w_st, gu, a, *, name, tm=512, comm=None):
    T, dm = h.shape
    hw = w_st.shape[2]
    tm = _pick(T, tm)

    def body(h_ref, wg_ref, wu_ref, gu_in, a_in, gu_ref, a_ref):
        hv = h_ref[...]
        g = jnp.dot(hv, wg_ref[0], preferred_element_type=F32)
        u = jnp.dot(hv, wu_ref[0], preferred_element_type=F32)
        gu_ref[:, :hw] = g.astype(gu_ref.dtype)
        gu_ref[:, hw:] = u.astype(gu_ref.dtype)
        a_ref[...] = (g * _sigmoid(g) * u).astype(a_ref.dtype)

    return _pcall(body, name=name, grid=(T // tm,),
                  in_specs=[_row_spec(tm, dm), pl.BlockSpec((1, dm, hw), lambda i: (1, 0, 0)),
                            pl.BlockSpec((1, dm, hw), lambda i: (3, 0, 0)), HBM_SPEC, HBM_SPEC],
                  out_specs=[pl.BlockSpec((tm, 2 * hw), lambda i: (i, 1)), pl.BlockSpec((tm, hw), lambda i: (i, 1))],
                  out_shape=[_sds(gu.shape, gu.dtype), _sds(a.shape, a.dtype)], aliases={3: 0, 4: 1},
                  comm=comm)(h, w_st, w_st, gu, a)


def _ffn_dact(df, gu, w_out, *, name, tm=512, comm=None):
    T, dm = df.shape
    hw = gu.shape[1] // 4
    tm = _pick(T, tm)

    def body(df_ref, gu_ref, wo_ref, dgu_ref):
        da = lax.dot_general(df_ref[...], wo_ref[...], (((1,), (1,)), ((), ())), preferred_element_type=F32)
        g = gu_ref[:, :hw].astype(F32)
        u = gu_ref[:, hw:].astype(F32)
        s = _sigmoid(g)
        dgu_ref[:, :hw] = (da * u * (s * (1.0 + g * (1.0 - s)))).astype(dgu_ref.dtype)
        dgu_ref[:, hw:] = (da * (g * s)).astype(dgu_ref.dtype)

    return _pcall(body, name=name, grid=(T // tm, 2),
                  in_specs=[pl.BlockSpec((tm, dm), lambda i, j: (i, 0)), pl.BlockSpec((tm, 2 * hw), lambda i, j: (i, j)),
                            pl.BlockSpec((hw, dm), lambda i, j: (j, 0))],
                  out_specs=pl.BlockSpec((tm, 2 * hw), lambda i, j: (i, j)), out_shape=_sds(gu.shape, MX),
                  comm=comm)(df, gu, w_out)


def _ffn_dh(dgu, w_st, *, name, tm=1024, comm=None):
    T = dgu.shape[0]
    _, dm, hw = w_st.shape
    tm = _pick(T, tm)

    def body(d_ref, w_ref, o_ref, acc_ref):
        q = pl.program_id(1)
        part = lax.dot_general(d_ref[...], w_ref[0], (((1,), (1,)), ((), ())), preferred_element_type=F32)

        @pl.when(q == 0)
        def _():
            acc_ref[...] = part

        @pl.when(jnp.logical_and(q > 0, q < 3))
        def _():
            acc_ref[...] += part

        @pl.when(q == 3)
        def _():
            o_ref[...] = acc_ref[...] + part

    return _pcall(body, name=name, grid=(T // tm, 4),
                  in_specs=[pl.BlockSpec((tm, hw), lambda i, q: (i, q)), pl.BlockSpec((1, dm, hw), lambda i, q: (_gu_shard(q), 0, 0))],
                  out_specs=pl.BlockSpec((tm, dm), lambda i, q: (i, 0)), out_shape=_sds((T, dm), F32),
                  scratch=[pltpu.VMEM((tm, dm), F32)], comm=comm)(dgu, w_st)


def _ffn_dw_in(h, dgu, *, name, rows=None, tm=512, comm=None):
    T, dm = h.shape
    hw = dgu.shape[1] // 4
    first, count = rows if rows is not None else (0, dm)
    tm = _pick(count, tm)
    skip = first // tm

    def body(h_ref, d_ref, o_ref):
        o_ref[0] = lax.dot_general(h_ref[...], d_ref[...], (((0,), (0,)), ((), ())), preferred_element_type=F32)

    return _pcall(body, name=name, grid=(4, count // tm),
                  in_specs=[pl.BlockSpec((T, tm), lambda q, i: (0, skip + i)), pl.BlockSpec((T, hw), lambda q, i: (0, q))],
                  out_specs=pl.BlockSpec((1, tm, hw), lambda q, i: (_gu_shard(q), i, 0)),
                  out_shape=_sds((4, count, hw), F32), comm=comm)(h, dgu)


def _rsq(x):
    return lax.rsqrt(jnp.mean(x * x, axis=-1, keepdims=True) + EPS)


def _sigmoid(x):
    return 1.0 / (1.0 + jnp.exp(-x))


def _row_spec(tm, w):
    return pl.BlockSpec((tm, w), lambda i: (i, 0))


def _mm_resid_norm(a, w, x_prev, mod3, gain, *, sub, coef, name, tm=512, comm=None):
    T, k = a.shape
    dm = w.shape[1]
    tm = _pick(T, tm)
    tps = (T // mod3.shape[0]) // tm

    def body(a_ref, w_ref, x_ref, mod_ref, n_ref, f_ref, xo_ref, h_ref):
        f = jnp.dot(a_ref[...], w_ref[...], preferred_element_type=F32)
        f_ref[...] = f
        x = x_ref[...] + coef * mod_ref[0, 3 * sub - 1:3 * sub, :] * f
        xo_ref[...] = x
        xn = x * _rsq(x) * n_ref[...]
        h_ref[...] = (xn * (1.0 + mod_ref[0, 3 * sub + 1:3 * sub + 2, :]) + mod_ref[0, 3 * sub:3 * sub + 1, :]).astype(h_ref.dtype)

    row = _row_spec(tm, dm)
    return _pcall(body, name=name, grid=(T // tm,),
                  in_specs=[_row_spec(tm, k), pl.BlockSpec((k, dm), lambda i: (0, 0)), row,
                            pl.BlockSpec((1, 9, dm), lambda i: (i // tps, 0, 0)), pl.BlockSpec((1, dm), lambda i: (0, 0))],
                  out_specs=[row, row, row], out_shape=[_sds((T, dm), F32), _sds((T, dm), F32), _sds((T, dm), MX)],
                  comm=comm)(a, w, x_prev, mod3, gain)


def _mm_loss(a, w, x2, tgt, mod3, *, name, tm=512):
    T, k = a.shape
    dm = w.shape[1]
    tm = _pick(T, tm)
    tps = (T // mod3.shape[0]) // tm
    mod_spec = pl.BlockSpec((1, 9, dm), lambda i: (i // tps, 0, 0))
    row = _row_spec(tm, dm)
    stat_spec = pl.BlockSpec((1, SUBLANE, dm), lambda i: (i // tps, 0, 0))
    loss_spec = pl.BlockSpec((SUBLANE, LANE), lambda i: (0, 0))

    def body(a_ref, w_ref, x_ref, t_ref, mod_ref, dy_ref, df_ref, st_ref, loss_ref):
        i = pl.program_id(0)
        g = mod_ref[0, 8:9, :]
        f = jnp.dot(a_ref[...], w_ref[...], preferred_element_type=F32)
        err = x_ref[...] + 0.5 * g * f - t_ref[...]
        dy = err * (1.0 / dm)
        dy_ref[...] = dy
        df_ref[...] = (0.5 * g * dy).astype(df_ref.dtype)
        dgate = jnp.sum(0.5 * dy * f, axis=0, keepdims=True)
        part = 0.5 * jnp.sum(jnp.sum(err * err, axis=0, keepdims=True), axis=1, keepdims=True) * (1.0 / dm)

        @pl.when(i % tps == 0)
        def _():
            st_ref[...] = jnp.zeros_like(st_ref)

        @pl.when(i == 0)
        def _():
            loss_ref[...] = jnp.zeros_like(loss_ref)

        st_ref[0, 0:1, :] += dgate
        loss_ref[...] += jnp.broadcast_to(part, loss_ref.shape)

    return _pcall(body, name=name, grid=(T // tm,),
                  in_specs=[_row_spec(tm, k), pl.BlockSpec((k, dm), lambda i: (0, 0)), row, row, mod_spec],
                  out_specs=[row, row, stat_spec, loss_spec],
                  out_shape=[_sds((T, dm), F32), _sds((T, dm), MX), _sds((mod3.shape[0], SUBLANE, dm), F32),
                             _sds((SUBLANE, LANE), F32)])(a, w, x2, tgt, mod3)


def _norm_bwd_tail(dhv, x_ref, dxi_ref, f_ref, mod_ref, n_ref, dx_ref, df_ref, st_ref, *, first, sub, coef):
    x = x_ref[...]
    r = _rsq(x)
    xhat = x * r
    n = n_ref[...]
    d_shift = jnp.sum(dhv, axis=0, keepdims=True)
    d_scale = jnp.sum(dhv * (xhat * n), axis=0, keepdims=True)
    dxn = dhv * (1.0 + mod_ref[0, 3 * sub + 1:3 * sub + 2, :])
    d_gain = jnp.sum(dxn * xhat, axis=0, keepdims=True)
    dxhat = dxn * n
    dx = dxi_ref[...] + r * (dxhat - xhat * jnp.mean(dxhat * xhat, axis=-1, keepdims=True))
    dx_ref[...] = dx

    @pl.when(first)
    def _():
        st_ref[...] = jnp.zeros_like(st_ref)

    st_ref[0, 0:1, :] += d_shift
    st_ref[0, 1:2, :] += d_scale
    st_ref[0, 2:3, :] += d_gain
    if f_ref is not None:
        st_ref[0, 3:4, :] += jnp.sum(coef * dx * f_ref[...], axis=0, keepdims=True)
        df_ref[...] = (coef * mod_ref[0, 3 * sub - 1:3 * sub, :] * dx).astype(df_ref.dtype)


def _ffn_dh_norm(dgu, w_st, x_cur, dx_in, f_prev, mod3, gain, *, sub, coef, name, tm=512, comm=None):
    T = dgu.shape[0]
    _, dm, hw = w_st.shape
    tm = _pick(T, tm)
    nb = mod3.shape[0]
    tps = (T // nb) // tm

    def body(d_ref, w_ref, x_ref, dxi_ref, f_ref, mod_ref, n_ref, dx_ref, df_ref, st_ref, acc_ref):
        i = pl.program_id(0)
        q = pl.program_id(1)
        part = lax.dot_general(d_ref[...], w_ref[0], (((1,), (1,)), ((), ())), preferred_element_type=F32)

        @pl.when(q == 0)
        def _():
            acc_ref[...] = part

        @pl.when(q > 0)
        def _():
            acc_ref[...] += part

        @pl.when(q == 3)
        def _():
            _norm_bwd_tail(acc_ref[...], x_ref, dxi_ref, f_ref, mod_ref, n_ref, dx_ref, df_ref, st_ref,
                           first=i % tps == 0, sub=sub, coef=coef)

    row = pl.BlockSpec((tm, dm), lambda i, q: (i, 0))
    return _pcall(body, name=name, grid=(T // tm, 4),
                  in_specs=[pl.BlockSpec((tm, hw), lambda i, q: (i, q)), pl.BlockSpec((1, dm, hw), lambda i, q: (_gu_shard(q), 0, 0)),
                            row, row, row, pl.BlockSpec((1, 9, dm), lambda i, q: (i // tps, 0, 0)),
                            pl.BlockSpec((1, dm), lambda i, q: (0, 0))],
                  out_specs=[row, row, pl.BlockSpec((1, SUBLANE, dm), lambda i, q: (i // tps, 0, 0))],
                  out_shape=[_sds((T, dm), F32), _sds((T, dm), MX), _sds((nb, SUBLANE, dm), F32)],
                  scratch=[pltpu.VMEM((tm, dm), F32)], comm=comm)(dgu, w_st, x_cur, dx_in, f_prev, mod3, gain)


def _mm_nt_norm(a, b, x_cur, dx_in, f_prev, mod3, gain, *, sub, coef, name, tm=512, comm=None):
    T, k = a.shape
    dm = b.shape[0]
    tm = _pick(T, tm)
    nb = mod3.shape[0]
    tps = (T // nb) // tm

    def body(a_ref, b_ref, x_ref, dxi_ref, f_ref, mod_ref, n_ref, dx_ref, df_ref, st_ref):
        dh = lax.dot_general(a_ref[...], b_ref[...], (((1,), (1,)), ((), ())), preferred_element_type=F32)
        _norm_bwd_tail(dh, x_ref, dxi_ref, f_ref, mod_ref, n_ref, dx_ref, df_ref, st_ref,
                       first=pl.program_id(0) % tps == 0, sub=sub, coef=coef)

    row = _row_spec(tm, dm)
    return _pcall(body, name=name, grid=(T // tm,),
                  in_specs=[_row_spec(tm, k), pl.BlockSpec((dm, k), lambda i: (0, 0)), row, row, row,
                            pl.BlockSpec((1, 9, dm), lambda i: (i // tps, 0, 0)), pl.BlockSpec((1, dm), lambda i: (0, 0))],
                  out_specs=[row, row, pl.BlockSpec((1, SUBLANE, dm), lambda i: (i // tps, 0, 0))],
                  out_shape=[_sds((T, dm), F32), _sds((T, dm), MX), _sds((nb, SUBLANE, dm), F32)],
                  comm=comm)(a, b, x_cur, dx_in, f_prev, mod3, gain)


def _bwd_block(x_cur, dh, dx_in, f_prev, mod3, gain, *, sub, coef, name, tm=256, comm=None):
    T, dm = x_cur.shape
    nb = mod3.shape[0]
    tps = (T // nb) // tm
    has_f = f_prev is not None
    mod_spec = pl.BlockSpec((1, 9, dm), lambda i: (i // tps, 0, 0))
    vec_spec = pl.BlockSpec((1, dm), lambda i: (0, 0))
    stat_spec = pl.BlockSpec((1, SUBLANE, dm), lambda i: (i // tps, 0, 0))
    row = _row_spec(tm, dm)

    def body(*refs):
        if has_f:
            x_ref, dh_ref, dxi_ref, f_ref, mod_ref, n_ref, dx_ref, df_ref, st_ref = refs
        else:
            x_ref, dh_ref, dxi_ref, mod_ref, n_ref, dx_ref, st_ref = refs
            f_ref = df_ref = None
        _norm_bwd_tail(dh_ref[...], x_ref, dxi_ref, f_ref, mod_ref, n_ref, dx_ref, df_ref, st_ref,
                       first=pl.program_id(0) % tps == 0, sub=sub, coef=coef)

    st_shape = _sds((nb, SUBLANE, dm), F32)
    if has_f:
        return _pcall(body, name=name, grid=(T // tm,), in_specs=[row, row, row, row, mod_spec, vec_spec],
                      out_specs=[row, row, stat_spec],
                      out_shape=[_sds((T, dm), F32), _sds((T, dm), MX), st_shape], comm=comm)(x_cur, dh, dx_in, f_prev, mod3, gain)
    return _pcall(body, name=name, grid=(T // tm,), in_specs=[row, row, row, mod_spec, vec_spec],
                  out_specs=[row, stat_spec], out_shape=[_sds((T, dm), F32), st_shape], comm=comm)(x_cur, dh, dx_in, mod3, gain)


def _mix_out_fwd(z, br_pool, attn, w_mla, w_out, x_prev, mod3, gain, *, sub, name, tm=512):
    T = z.shape[0]
    dm = w_out.shape[1]
    tm = _pick(T, tm)
    tps = (T // mod3.shape[0]) // tm
    whole = lambda a: pl.BlockSpec(a.shape, lambda i: (0, 0))

    def body(z_ref, bp_ref, at_ref, wm_ref, wo_ref, x_ref, mod_ref, n_ref, bm_ref, mg_ref, mo_ref, xo_ref, h_ref):
        bm = jnp.dot(at_ref[...], wm_ref[...], preferred_element_type=F32).astype(MX)
        bm_ref[...] = bm
        gp = z_ref[:, Z_GP:Z_GP + D].astype(F32)
        gm = z_ref[:, Z_GM:Z_GM + D].astype(F32)
        merged = (_sigmoid(gp) * bp_ref[...] + _sigmoid(gm) * bm).astype(MX)
        mg_ref[...] = merged
        mo = jnp.dot(merged, wo_ref[...], preferred_element_type=F32)
        mo_ref[...] = mo
        x = x_ref[...] + mod_ref[0, 3 * sub - 1:3 * sub, :] * mo
        xo_ref[...] = x
        xn = x * _rsq(x) * n_ref[...]
        h_ref[...] = (xn * (1.0 + mod_ref[0, 3 * sub + 1:3 * sub + 2, :]) + mod_ref[0, 3 * sub:3 * sub + 1, :]).astype(h_ref.dtype)

    row = _row_spec(tm, dm)
    return _pcall(body, name=name, grid=(T // tm,),
                  in_specs=[_row_spec(tm, Z_W), row, _row_spec(tm, attn.shape[1]), whole(w_mla), whole(w_out), row,
                            pl.BlockSpec((1, 9, dm), lambda i: (i // tps, 0, 0)), pl.BlockSpec((1, dm), lambda i: (0, 0))],
                  out_specs=[row, row, row, row, row],
                  out_shape=[_sds((T, dm), MX), _sds((T, dm), MX), _sds((T, dm), F32), _sds((T, dm), F32), _sds((T, dm), MX)],
                  )(z, br_pool, attn, w_mla, w_out, x_prev, mod3, gain)


def _mix_out_bwd(dmo, w_out, z, br_pool, br_mla, w_mla, *, name, tm=512):
    T = z.shape[0]
    tm = _pick(T, tm)
    whole = lambda a: pl.BlockSpec(a.shape, lambda i: (0, 0))
    nt = (((1,), (1,)), ((), ()))

    def body(dmo_ref, wo_ref, z_ref, bp_ref, bm_ref, wm_ref, dbp_ref, dbm_ref, dg_ref, dat_ref):
        dm = lax.dot_general(dmo_ref[...], wo_ref[...], nt, preferred_element_type=F32)
        sp = _sigmoid(z_ref[:, Z_GP:Z_GP + D].astype(F32))
        sm = _sigmoid(z_ref[:, Z_GM:Z_GM + D].astype(F32))
        dbp_ref[...] = (dm * sp).astype(dbp_ref.dtype)
        dbm = (dm * sm).astype(MX)
        dbm_ref[...] = dbm
        dg_ref[:, :D] = (dm * bp_ref[...].astype(F32) * sp * (1.0 - sp)).astype(dg_ref.dtype)
        dg_ref[:, D:] = (dm * bm_ref[...].astype(F32) * sm * (1.0 - sm)).astype(dg_ref.dtype)
        dat_ref[...] = lax.dot_general(dbm, wm_ref[...], nt, preferred_element_type=F32)

    row = _row_spec(tm, D)
    return _pcall(body, name=name, grid=(T // tm,),
                  in_specs=[row, whole(w_out), _row_spec(tm, Z_W), row, row, whole(w_mla)],
                  out_specs=[row, row, _row_spec(tm, 2 * D), _row_spec(tm, w_mla.shape[0])],
                  out_shape=[_sds((T, D), MX), _sds((T, D), MX), _sds((T, 2 * D), MX), _sds((T, w_mla.shape[0]), F32)],
                  )(dmo, w_out, z, br_pool, br_mla, w_mla)


def _shift_down(x, k, row):
    return jnp.where(row >= k, pltpu.roll(x, k, 0), 0.0)


def _shift_up(x, k, row, n):
    return jnp.where(row < n - k, pltpu.roll(x, n - k, 0), 0.0)


def _pool_fwd(z, pool_grp, pool_scale, *, nb, name):
    T = z.shape[0]
    S = T // nb
    blk = pl.BlockSpec((S, LANE), lambda b, g: (b, g))

    def body(u_ref, w_ref, s_ref, pooled_ref, mixed_ref, scaled_ref):
        g = pl.program_id(1)
        u = u_ref[...].astype(F32)
        row = lax.broadcasted_iota(jnp.int32, u.shape, 0)
        s2 = u + _shift_down(u, 1, row)
        s4 = s2 + _shift_down(s2, 2, row)
        s8 = s4 + _shift_down(s4, 4, row)
        s16 = s8 + _shift_down(s8, 8, row)
        win = jnp.where(g == 0, s2, jnp.where(g == 1, s4, jnp.where(g == 2, s8, s16)))
        width = lax.shift_left(jnp.int32(2), g)
        cnt = jnp.minimum(row + 1, width).astype(F32)
        pooled = (win / cnt - u).astype(MX)
        pooled_ref[...] = pooled
        mixed = jnp.dot(pooled, w_ref[0], preferred_element_type=F32)
        mixed_ref[...] = mixed
        scaled_ref[...] = (mixed * s_ref[...]).astype(scaled_ref.dtype)

    return _pcall(body, name=name, grid=(nb, POOL_G),
                  in_specs=[blk, pl.BlockSpec((1, LANE, LANE), lambda b, g: (g, 0, 0)),
                            pl.BlockSpec((1, LANE), lambda b, g: (0, g))],
                  out_specs=[blk, blk, blk],
                  out_shape=[_sds((T, POOL_W), MX), _sds((T, POOL_W), F32), _sds((T, POOL_W), MX)])(z, pool_grp, pool_scale)


def _pool_bwd(dscaled, mixed, pooled, pool_grp, pool_scale, *, nb, name):
    T = dscaled.shape[0]
    S = T // nb
    blk = pl.BlockSpec((S, LANE), lambda g, b: (b, g))

    def body(ds_ref, mixed_ref, pooled_ref, w_ref, s_ref, du_ref, dw_ref, dsc_ref):
        g = pl.program_id(0)
        b = pl.program_id(1)
        ds = ds_ref[...]
        dsc_ref[0] = jnp.sum(ds * mixed_ref[...], axis=0, keepdims=True)
        dmixed = (ds * s_ref[...]).astype(MX)
        dw = lax.dot_general(pooled_ref[...], dmixed, (((0,), (0,)), ((), ())), preferred_element_type=F32)

        @pl.when(b == 0)
        def _():
            dw_ref[0] = dw

        @pl.when(b > 0)
        def _():
            dw_ref[0] += dw
        dpooled = lax.dot_general(dmixed, w_ref[0], (((1,), (1,)), ((), ())), preferred_element_type=F32)
        row = lax.broadcasted_iota(jnp.int32, dpooled.shape, 0)
        width = lax.shift_left(jnp.int32(2), g)
        q = dpooled / jnp.minimum(row + 1, width).astype(F32)
        r2 = q + _shift_up(q, 1, row, S)
        r4 = r2 + _shift_up(r2, 2, row, S)
        r8 = r4 + _shift_up(r4, 4, row, S)
        r16 = r8 + _shift_up(r8, 8, row, S)
        win = jnp.where(g == 0, r2, jnp.where(g == 1, r4, jnp.where(g == 2, r8, r16)))
        du_ref[...] = (win - dpooled).astype(du_ref.dtype)

    return _pcall(body, name=name, grid=(POOL_G, nb),
                  in_specs=[blk, blk, blk, pl.BlockSpec((1, LANE, LANE), lambda g, b: (g, 0, 0)),
                            pl.BlockSpec((1, LANE), lambda g, b: (0, g))],
                  out_specs=[blk, pl.BlockSpec((1, LANE, LANE), lambda g, b: (g, 0, 0)),
                             pl.BlockSpec((1, 1, LANE), lambda g, b: (b, 0, g))],
                  out_shape=[_sds((T, POOL_W), MX), _sds((POOL_G, LANE, LANE), F32), _sds((nb, 1, POOL_W), F32)],
                  )(dscaled, mixed, pooled, pool_grp, pool_scale)


def _lane_masks(shape):
    lane = lax.broadcasted_iota(jnp.int32, shape, len(shape) - 1)
    m_n = lane < NOPE
    m_r = jnp.logical_and(lane >= KR_LANE, lane < KR_LANE + ROPE)
    first_half = lane < KR_LANE + ROPE // 2
    return m_n, m_r, first_half


def _rot(y, first_half):
    return jnp.where(first_half, -pltpu.roll(y, LANE - ROPE // 2, 1), pltpu.roll(y, ROPE // 2, 1))


def _rot_t(v, first_half, m_r):
    return jnp.where(m_r, jnp.where(first_half, pltpu.roll(v, LANE - ROPE // 2, 1), -pltpu.roll(v, ROPE // 2, 1)), 0.0)


def _mla_in_fwd(z, w_q, w_kv, qa_gain, kva_gain, cos, sin, q_gain, k_gain, *, name, tm=256):
    T = z.shape[0]
    slab = pl.BlockSpec((tm, LANE), lambda i: (i, 0))
    vec = pl.BlockSpec((1, LANE), lambda i: (0, 0))
    whole = lambda a: pl.BlockSpec(a.shape, lambda i: (0, 0))

    def body(z_ref, wq_ref, wkv_ref, qa_ref, kva_ref, cos_ref, sin_ref, qg_ref, kg_ref,
             qn_ref, kvn_ref, qp_ref, kvp_ref, q_ref, k_ref, v_ref):
        ql = z_ref[:, Z_QL:Z_QL + QL].astype(F32)
        kvl = z_ref[:, Z_KV:Z_KV + KVL].astype(F32)
        qn = (ql * _rsq(ql) * qa_ref[...]).astype(MX)
        kvn = (kvl * _rsq(kvl) * kva_ref[...]).astype(MX)
        qn_ref[...] = qn
        kvn_ref[...] = kvn
        qp = jnp.dot(qn, wq_ref[...], preferred_element_type=F32)
        kvp = jnp.dot(kvn, wkv_ref[...], preferred_element_type=F32)
        qp_ref[...] = qp
        kvp_ref[...] = kvp
        m_n, m_r, first_half = _lane_masks((tm, LANE))
        c = cos_ref[...]
        s = sin_ref[...]
        qg = qg_ref[...]
        kg = kg_ref[...]
        xr = z_ref[:, Z_KR:Z_KR + LANE].astype(F32)
        rr = lax.rsqrt(jnp.sum(xr * xr, axis=-1, keepdims=True) * (1.0 / ROPE) + EPS)
        yr = xr * rr * kg
        kr = jnp.where(m_r, yr * c + _rot(yr, first_half) * s, 0.0)
        for h in range(NH):
            x = qp[:, h * LANE:(h + 1) * LANE]
            x2 = x * x
            rn = lax.rsqrt(jnp.sum(jnp.where(m_n, x2, 0.0), axis=-1, keepdims=True) * (1.0 / NOPE) + EPS)
            rq = lax.rsqrt(jnp.sum(jnp.where(m_r, x2, 0.0), axis=-1, keepdims=True) * (1.0 / ROPE) + EPS)
            y = x * jnp.where(m_n, rn, jnp.where(m_r, rq, 0.0)) * qg
            q_ref[:, h * LANE:(h + 1) * LANE] = ((y * c + _rot(y, first_half) * s) * ATTN_SCALE).astype(q_ref.dtype)
            xk = kvp[:, h * LANE:(h + 1) * LANE]
            rk = lax.rsqrt(jnp.sum(jnp.where(m_n, xk * xk, 0.0), axis=-1, keepdims=True) * (1.0 / NOPE) + EPS)
            k_ref[:, h * LANE:(h + 1) * LANE] = (jnp.where(m_n, xk * rk * kg, 0.0) + kr).astype(k_ref.dtype)
        v_ref[...] = kvp[:, NH * LANE:].astype(v_ref.dtype)

    return _pcall(body, name=name, grid=(T // tm,),
                  in_specs=[_row_spec(tm, LAT_W), whole(w_q), whole(w_kv), whole(qa_gain), whole(kva_gain), slab, slab, vec, vec],
                  out_specs=[_row_spec(tm, QL), _row_spec(tm, KVL), _row_spec(tm, NH * LANE), _row_spec(tm, NH * LANE + NH * NOPE),
                             _row_spec(tm, NH * LANE), _row_spec(tm, NH * LANE), _row_spec(tm, NH * NOPE)],
                  out_shape=[_sds((T, QL), MX), _sds((T, KVL), MX), _sds((T, NH * LANE), F32), _sds((T, NH * LANE + NH * NOPE), F32),
                             _sds((T, NH * LANE), MX), _sds((T, NH * LANE), MX), _sds((T, NH * NOPE), MX)],
                  )(z, w_q, w_kv, qa_gain, kva_gain, cos, sin, q_gain, k_gain)


def _mla_in_bwd(dq, dk, dv, qp, kvp, z, w_q, w_kv, qa_gain, kva_gain, cos, sin, q_gain, k_gain, *, nb, name, tm=256):
    T = dq.shape[0]
    tps = (T // nb) // tm
    slab = pl.BlockSpec((tm, LANE), lambda i: (i, 0))
    vec = pl.BlockSpec((1, LANE), lambda i: (0, 0))
    whole = lambda a: pl.BlockSpec(a.shape, lambda i: (0, 0))

    def latent_bwd(x, dy, gain):
        r = _rsq(x)
        xhat = x * r
        dxhat = dy * gain
        return r * (dxhat - xhat * jnp.mean(dxhat * xhat, axis=-1, keepdims=True)), jnp.sum(dy * xhat, axis=0, keepdims=True)

    def body(dq_ref, dk_ref, dv_ref, qp_ref, kvp_ref, z_ref, wq_ref, wkv_ref, qa_ref, kva_ref, cos_ref, sin_ref, qg_ref, kg_ref,
             dqp_ref, dkvp_ref, dkr_ref, dql_ref, dkvl_ref, st_ref, sq_ref, sk_ref):
        i = pl.program_id(0)
        qp, kvp = qp_ref, kvp_ref
        m_n, m_r, first_half = _lane_masks((tm, LANE))
        c = cos_ref[...]
        s = sin_ref[...]
        qg = qg_ref[...]
        kg = kg_ref[...]
        dqg = jnp.zeros((1, LANE), F32)
        dkg = jnp.zeros((1, LANE), F32)
        dkr_sum = jnp.zeros((tm, LANE), F32)
        for h in range(NH):
            x = qp[:, h * LANE:(h + 1) * LANE]
            x2 = x * x
            rn = lax.rsqrt(jnp.sum(jnp.where(m_n, x2, 0.0), axis=-1, keepdims=True) * (1.0 / NOPE) + EPS)
            rq = lax.rsqrt(jnp.sum(jnp.where(m_r, x2, 0.0), axis=-1, keepdims=True) * (1.0 / ROPE) + EPS)
            rfac = jnp.where(m_n, rn, jnp.where(m_r, rq, 0.0))
            xhat = x * rfac
            do = dq_ref[:, h * LANE:(h + 1) * LANE] * ATTN_SCALE
            dy = do * c + _rot_t(do * s, first_half, m_r)
            dqg = dqg + jnp.sum(dy * xhat, axis=0, keepdims=True)
            dxhat = dy * qg
            t = dxhat * xhat
            mean_n = jnp.sum(jnp.where(m_n, t, 0.0), axis=-1, keepdims=True) * (1.0 / NOPE)
            mean_r = jnp.sum(jnp.where(m_r, t, 0.0), axis=-1, keepdims=True) * (1.0 / ROPE)
            dqp_ref[:, h * LANE:(h + 1) * LANE] = (
                rfac * (dxhat - xhat * jnp.where(m_n, mean_n, jnp.where(m_r, mean_r, 0.0)))).astype(dqp_ref.dtype)

            xk = kvp[:, h * LANE:(h + 1) * LANE]
            rk = lax.rsqrt(jnp.sum(jnp.where(m_n, xk * xk, 0.0), axis=-1, keepdims=True) * (1.0 / NOPE) + EPS)
            khat = jnp.where(m_n, xk * rk, 0.0)
            dko = dk_ref[:, h * LANE:(h + 1) * LANE]
            dkn = jnp.where(m_n, dko, 0.0)
            dkg = dkg + jnp.sum(dkn * khat, axis=0, keepdims=True)
            dkhat = dkn * kg
            mean_k = jnp.sum(dkhat * khat, axis=-1, keepdims=True) * (1.0 / NOPE)
            dkvp_ref[:, h * LANE:(h + 1) * LANE] = jnp.where(m_n, rk * (dkhat - khat * mean_k), 0.0).astype(dkvp_ref.dtype)
            dkr_sum = dkr_sum + jnp.where(m_r, dko, 0.0)
        dkvp_ref[:, NH * LANE:] = dv_ref[...].astype(dkvp_ref.dtype)

        xr = z_ref[:, Z_KR:Z_KR + LANE].astype(F32)
        rr = lax.rsqrt(jnp.sum(xr * xr, axis=-1, keepdims=True) * (1.0 / ROPE) + EPS)
        rhat = xr * rr
        dyr = dkr_sum * c + _rot_t(dkr_sum * s, first_half, m_r)
        dkg = dkg + jnp.sum(dyr * rhat, axis=0, keepdims=True)
        drhat = dyr * kg
        mean_kr = jnp.sum(drhat * rhat, axis=-1, keepdims=True) * (1.0 / ROPE)
        dkr_ref[...] = jnp.where(m_r, rr * (drhat - rhat * mean_kr), 0.0).astype(dkr_ref.dtype)

        nt = (((1,), (1,)), ((), ()))
        dqn = lax.dot_general(dqp_ref[...], wq_ref[...], nt, preferred_element_type=F32)
        dkvn = lax.dot_general(dkvp_ref[...], wkv_ref[...], nt, preferred_element_type=F32)
        dql, dqa = latent_bwd(z_ref[:, Z_QL:Z_QL + QL].astype(F32), dqn, qa_ref[...])
        dkvl, dkva = latent_bwd(z_ref[:, Z_KV:Z_KV + KVL].astype(F32), dkvn, kva_ref[...])
        dql_ref[...] = dql.astype(dql_ref.dtype)
        dkvl_ref[...] = dkvl.astype(dkvl_ref.dtype)

        @pl.when(i % tps == 0)
        def _():
            st_ref[...] = jnp.zeros_like(st_ref)
            sq_ref[...] = jnp.zeros_like(sq_ref)
            sk_ref[...] = jnp.zeros_like(sk_ref)

        st_ref[0, 0:1, :] += dqg
        st_ref[0, 1:2, :] += dkg
        sq_ref[0, 0:1, :] += dqa
        sk_ref[0, 0:1, :] += dkva

    stat = lambda w: pl.BlockSpec((1, SUBLANE, w), lambda i: (i // tps, 0, 0))
    return _pcall(body, name=name, grid=(T // tm,),
                  in_specs=[_row_spec(tm, NH * LANE), _row_spec(tm, NH * LANE), _row_spec(tm, NH * NOPE),
                            _row_spec(tm, NH * LANE), _row_spec(tm, NH * LANE + NH * NOPE), _row_spec(tm, LAT_W),
                            whole(w_q), whole(w_kv), whole(qa_gain), whole(kva_gain), slab, slab, vec, vec],
                  out_specs=[_row_spec(tm, NH * LANE), _row_spec(tm, NH * LANE + NH * NOPE), slab, _row_spec(tm, QL),
                             _row_spec(tm, KVL), stat(LANE), stat(QL), stat(KVL)],
                  out_shape=[_sds((T, NH * LANE), MX), _sds((T, NH * LANE + NH * NOPE), MX), _sds((T, LANE), MX),
                             _sds((T, QL), MX), _sds((T, KVL), MX), _sds((nb, SUBLANE, LANE), F32),
                             _sds((nb, SUBLANE, QL), F32), _sds((nb, SUBLANE, KVL), F32)],
                  )(dq, dk, dv, qp, kvp, z, w_q, w_kv, qa_gain, kva_gain, cos, sin, q_gain, k_gain)


def _lower_triangle(t):
    return lax.broadcasted_iota(jnp.int32, (t, t), 1) <= lax.broadcasted_iota(jnp.int32, (t, t), 0)


def _attn_fwd(q, k, v, *, nb, name, tq=512, comm=None):
    T = q.shape[0]
    S = T // nb
    tq = _pick(S, tq)
    nq = S // tq
    tk = tq
    npair = NH // 2

    def body(q_ref, k_ref, v_ref, o_ref, lse_ref):
        qi = pl.program_id(2)
        lane = lax.broadcasted_iota(jnp.int32, (tq, LANE), 1)
        qs = [q_ref[:, hh * LANE:(hh + 1) * LANE] for hh in range(2)]

        def block(j, carry, diagonal):
            k0 = pl.multiple_of(j * tk, tk)
            vb = v_ref[pl.ds(k0, tk), :]
            new = []
            for hh in range(2):
                m, l, acc = carry[hh]
                kb = k_ref[pl.ds(k0, tk), hh * LANE:(hh + 1) * LANE]
                s = lax.dot_general(qs[hh], kb, (((1,), (1,)), ((), ())), preferred_element_type=F32)
                if diagonal:
                    s = jnp.where(_lower_triangle(tq), s, NEG)
                m_new = jnp.maximum(m, jnp.max(s, axis=-1, keepdims=True))
                p = jnp.exp(s - m_new)
                alpha = jnp.exp(m - m_new)
                l = alpha * l + jnp.sum(p, axis=-1, keepdims=True)
                acc = alpha * acc + jnp.dot(p.astype(MX), vb, preferred_element_type=F32)
                new.append((m_new, l, acc))
            return tuple(new)

        init = tuple((jnp.full((tq, 1), NEG, F32), jnp.zeros((tq, 1), F32), jnp.zeros((tq, LANE), F32)) for _ in range(2))
        carry = lax.fori_loop(0, qi, lambda j, c: block(j, c, False), init)
        (m0, l0, acc0), (m1, l1, acc1) = block(qi, carry, True)
        o_ref[...] = jnp.where(lane < NOPE, acc0 / l0, acc1 / l1).astype(o_ref.dtype)
        lse_ref[...] = jnp.where(lane < NOPE, m0 + jnp.log(l0), m1 + jnp.log(l1))

    return _pcall(body, name=name, grid=(nb, npair, nq),
                  in_specs=[pl.BlockSpec((tq, 2 * LANE), lambda b, p, i: (b * nq + i, p)),
                            pl.BlockSpec((S, 2 * LANE), lambda b, p, i: (b, p)),
                            pl.BlockSpec((S, LANE), lambda b, p, i: (b, p))],
                  out_specs=[pl.BlockSpec((tq, LANE), lambda b, p, i: (b * nq + i, p)),
                             pl.BlockSpec((tq, LANE), lambda b, p, i: (b * nq + i, p))],
                  out_shape=[_sds((T, NH * NOPE), MX), _sds((T, NH * NOPE), F32)], comm=comm)(q, k, v)


def _attn_bwd(q, k, v, o, lse, do, *, nb, name, tq=512, comm=None):
    T = q.shape[0]
    S = T // nb
    tq = _pick(S, tq)
    nq = S // tq
    tk = tq
    npair = NH // 2

    def body(q_ref, k_ref, v_ref, o_ref, lse_ref, do_ref, dq_ref, dk_ref, dv_ref, delta_ref):
        lane = lax.broadcasted_iota(jnp.int32, (tq, LANE), 1)
        first = lane < NOPE
        dq_ref[...] = jnp.zeros_like(dq_ref)

        def delta_step(qi, _):
            q0 = pl.multiple_of(qi * tq, tq)
            prod = do_ref[pl.ds(q0, tq), :] * o_ref[pl.ds(q0, tq), :].astype(F32)
            d0 = jnp.sum(jnp.where(first, prod, 0.0), axis=-1, keepdims=True)
            d1 = jnp.sum(jnp.where(first, 0.0, prod), axis=-1, keepdims=True)
            delta_ref[pl.ds(q0, tq), :] = jnp.where(first, d0, d1)
            return 0

        lax.fori_loop(0, nq, delta_step, 0)

        def kv_step(kj, _):
            k0 = pl.multiple_of(kj * tk, tk)
            kbs = [k_ref[pl.ds(k0, tk), hh * LANE:(hh + 1) * LANE] for hh in range(2)]
            vb = v_ref[pl.ds(k0, tk), :]

            def q_block(qi, carry, diagonal):
                dk0, dk1, dv = carry
                dks = [dk0, dk1]
                q0 = pl.multiple_of(qi * tq, tq)
                dov = do_ref[pl.ds(q0, tq), :]
                lse_v = lse_ref[pl.ds(q0, tq), :]
                delta_v = delta_ref[pl.ds(q0, tq), :]
                for hh in range(2):
                    qh = q_ref[pl.ds(q0, tq), hh * LANE:(hh + 1) * LANE]
                    dob = jnp.where(first if hh == 0 else jnp.logical_not(first), dov, 0.0).astype(MX)
                    s = lax.dot_general(qh, kbs[hh], (((1,), (1,)), ((), ())), preferred_element_type=F32)
                    p = jnp.exp(s - lse_v[:, hh * NOPE:hh * NOPE + 1])
                    if diagonal:
                        p = jnp.where(_lower_triangle(tq), p, 0.0)
                    dp = lax.dot_general(dob, vb, (((1,), (1,)), ((), ())), preferred_element_type=F32)
                    ds = (p * (dp - delta_v[:, hh * NOPE:hh * NOPE + 1])).astype(MX)
                    dks[hh] = dks[hh] + lax.dot_general(ds, qh, (((0,), (0,)), ((), ())), preferred_element_type=F32)
                    dv = dv + lax.dot_general(p.astype(MX), dob, (((0,), (0,)), ((), ())), preferred_element_type=F32)
                    dq_ref[pl.ds(q0, tq), hh * LANE:(hh + 1) * LANE] += jnp.dot(ds, kbs[hh], preferred_element_type=F32)
                return dks[0], dks[1], dv

            zero = jnp.zeros((tk, LANE), F32)
            carry = q_block(kj, (zero, zero, zero), True)
            dk0, dk1, dv = lax.fori_loop(kj + 1, nq, lambda qi, c: q_block(qi, c, False), carry)
            dk_ref[pl.ds(k0, tk), 0:LANE] = dk0
            dk_ref[pl.ds(k0, tk), LANE:2 * LANE] = dk1
            dv_ref[pl.ds(k0, tk), :] = dv
            return 0

        lax.fori_loop(0, nq, kv_step, 0)

    pair256 = pl.BlockSpec((S, 2 * LANE), lambda b, p: (b, p))
    pair128 = pl.BlockSpec((S, LANE), lambda b, p: (b, p))
    return _pcall(body, name=name, grid=(nb, npair),
                  in_specs=[pair256, pair256, pair128, pair128, pair128, pair128],
                  out_specs=[pair256, pair256, pair128],
                  out_shape=[_sds((T, NH * LANE), F32), _sds((T, NH * LANE), F32), _sds((T, NH * NOPE), F32)],
                  scratch=[pltpu.VMEM((S, LANE), F32)], comm=comm)(q, k, v, o, lse, do)


def _run(plan, fn, *args, name, **kw):
    rider = plan.rider(name)
    if rider is None:
        return fn(*args, name=name, **kw)
    outs, landed = fn(*args, name=name, comm=rider, **kw)
    plan.landed(name, landed)
    return outs


def _layer_fwd_bwd(x, tgt, mod3, cos, sin, plan, P):
    nb = mod3.shape[0]
    W = plan.W
    h1, gu1, a1 = _run(plan, _ffn_up_first, x, mod3, P["norm_ffn1"], W["ffn1_in"], sub=0, name="ffn1_up_a")
    gu1, a1 = _run(plan, _ffn_up_second, h1, W["ffn1_in"], gu1, a1, name="ffn1_up_b")
    f1, x1, h2 = _run(plan, _mm_resid_norm, a1, W["ffn1_out"], x, mod3, P["norm_mix"], sub=1, coef=0.5, name="ffn1_out")
    z = _run(plan, _mm, h2, W["w_in"], mode="nn", out_dtype=MX, name="mix_in", tn=1664)
    pooled, mixed, scaled = _pool_fwd(z, W["pool_grp"], P["pool_scale"], nb=nb, name="pool_fwd")
    br_pool = _mm(scaled, W["pool_proj"], mode="nn", out_dtype=MX, name="pool_proj")
    qn, kvn, qp, kvp, q, k, v = _mla_in_fwd(z, W["q_up"], W["kv_up"], P["q_a_norm"], P["kv_a_norm"], cos, sin,
                                            P["q_gain"], P["k_gain"], name="mla_in_fwd")
    attn, lse = _run(plan, _attn_fwd, q, k, v, nb=nb, name="attn_fwd")
    br_mla, merged, mo, x2, h3 = _mix_out_fwd(z, br_pool, attn, W["mla_proj"], W["w_out"], x1, mod3, P["norm_ffn2"], sub=2,
                                              name="mix_out", tm=256)
    gu2, a2 = _ffn_up(h3, W["ffn2_in"], name="ffn2_up")
    dy, df2, st_fin, loss = _mm_loss(a2, W["ffn2_out"], x2, tgt, mod3, name="ffn2_out_loss")

    plan.grad("ffn2_out", _dw(a2, df2, name="d_ffn2_out"))
    dgu2 = _ffn_dact(df2, gu2, W["ffn2_out"], name="ffn2_dact")
    plan.grad("ffn2_in", _ffn_dw_in(h3, dgu2, name="d_ffn2_in"))
    dx2, dmo, st3 = _run(plan, _ffn_dh_norm, dgu2, W["ffn2_in"], x2, dy, mo, mod3, P["norm_ffn2"], sub=2, coef=1.0,
                         name="d_ffn2_h")
    plan.grad("w_out", _dw(merged, dmo, name="d_mix_out"))
    dbr_pool, dbr_mla, dgates, dattn = _mix_out_bwd(dmo, W["w_out"], z, br_pool, br_mla, W["mla_proj"], name="mix_out_bwd", tm=256)
    plan.grad("pool_proj", _dw(scaled, dbr_pool, name="d_pool_proj"))
    dscaled = _mm(dbr_pool, W["pool_proj"], mode="nt", out_dtype=F32, name="d_pool_scaled")
    du_pool, d_pool_grp, d_pool_scale = _pool_bwd(dscaled, mixed, pooled, W["pool_grp"], P["pool_scale"], nb=nb, name="pool_bwd")
    plan.grad("mla_proj", _dw(attn, dbr_mla, name="d_mla_proj"))
    dq, dk, dv = _run(plan, _attn_bwd, q, k, v, attn, lse, dattn, nb=nb, name="attn_bwd")
    dqp, dkvp, dkr, dql, dkvl, st_prep, st_q, st_kv = _mla_in_bwd(
        dq, dk, dv, qp, kvp, z, W["q_up"], W["kv_up"], P["q_a_norm"], P["kv_a_norm"], cos, sin, P["q_gain"], P["k_gain"],
        nb=nb, name="mla_in_bwd")
    plan.grad("q_up", _dw(qn, dqp, name="d_q_up"))
    plan.grad("kv_up", _dw(kvn, dkvp, name="d_kv_up"))
    dz = jnp.concatenate([du_pool, dql, dkvl, dkr, dgates], axis=1)
    plan.grad("w_in", _run(plan, _dw, h2, dz, name="d_mix_in", tm=256, tn=1664))
    dx1, df1, st2 = _run(plan, _mm_nt_norm, dz, W["w_in"], x1, dx2, f1, mod3, P["norm_mix"], sub=1, coef=0.5, name="d_mix_h")
    plan.grad("ffn1_out", _run(plan, _dw, a1, df1, name="d_ffn1_out"))
    dgu1 = _run(plan, _ffn_dact, df1, gu1, W["ffn1_out"], name="ffn1_dact")
    half = x.shape[1] // 2
    plan.grad("ffn1_in@0", _run(plan, _ffn_dw_in, h1, dgu1, rows=(0, half), name="d_ffn1_in_a"))
    plan.grad("ffn1_in@1", _run(plan, _ffn_dw_in, h1, dgu1, rows=(half, half), name="d_ffn1_in_b"))
    dh1 = _run(plan, _ffn_dh, dgu1, W["ffn1_in"], name="d_ffn1_h")
    grad_x, st1 = _run(plan, _bwd_block, x, dh1, dx1, None, mod3, P["norm_ffn1"], sub=0, coef=0.0, name="bwd_norm1")

    return loss, grad_x, (st1, st2, st3, st_fin, st_q, st_kv, st_prep, d_pool_scale), d_pool_grp


def _w_in_to_kernel(w):
    k = w.shape[0]
    zeros = lambda n: jnp.zeros((k, n), w.dtype)
    return jnp.concatenate([w[:, 0:1152], zeros(KR_LANE), w[:, 1152:1184], zeros(LANE - KR_LANE - ROPE), w[:, 1184:]], axis=1)


def _w_in_from_kernel(g):
    return jnp.concatenate([g[:, 0:1152], g[:, Z_KR + KR_LANE:Z_KR + KR_LANE + ROPE], g[:, Z_GP:]], axis=1)


def _q_up_to_kernel(w):
    k = w.shape[0]
    return jnp.pad(w.reshape(k, NH, NOPE + ROPE), ((0, 0), (0, 0), (0, LANE - NOPE - ROPE))).reshape(k, NH * LANE)


def _q_up_from_kernel(g):
    k = g.shape[0]
    return g.reshape(k, NH, LANE)[:, :, :NOPE + ROPE].reshape(k, NH * (NOPE + ROPE))


def _kv_up_to_kernel(w):
    k = w.shape[0]
    w3 = w.reshape(k, NH, 2 * NOPE)
    kpart = jnp.pad(w3[:, :, :NOPE], ((0, 0), (0, 0), (0, LANE - NOPE))).reshape(k, NH * LANE)
    return jnp.concatenate([kpart, w3[:, :, NOPE:].reshape(k, NH * NOPE)], axis=1)


def _kv_up_from_kernel(g):
    k = g.shape[0]
    kpart = g[:, :NH * LANE].reshape(k, NH, LANE)[:, :, :NOPE]
    vpart = g[:, NH * LANE:].reshape(k, NH, NOPE)
    return jnp.concatenate([kpart, vpart], axis=2).reshape(k, NH * 2 * NOPE)


def _gain_slab(nope, rope):
    return jnp.concatenate([nope, rope, jnp.zeros((1, LANE - NOPE - ROPE), nope.dtype)], axis=1)


def _rope_tables(positions):
    inv_freq = 10000.0 ** (-jnp.arange(0, ROPE, 2, dtype=F32) / ROPE)
    ang = positions.astype(F32)[:, None] * inv_freq
    ang = jnp.concatenate([ang, ang], axis=-1)
    t = positions.shape[0]
    cos = jnp.concatenate([jnp.ones((t, KR_LANE), F32), jnp.cos(ang), jnp.ones((t, LANE - KR_LANE - ROPE), F32)], axis=1)
    sin = jnp.concatenate([jnp.zeros((t, KR_LANE), F32), jnp.sin(ang), jnp.zeros((t, LANE - KR_LANE - ROPE), F32)], axis=1)
    return cos, sin


def _coords():
    return lax.axis_index("x"), lax.axis_index("y"), lax.axis_index("c")


HBM_SPEC = pl.BlockSpec(memory_space=pl.ANY)
VMEM_SPEC = pl.BlockSpec(memory_space=pltpu.VMEM)


CHIP_RELS = ((1, 0), (0, 1), (1, 1))


def _gather_comm(shards):
    n = len(shards)

    def ici(ins, outs, sems, a, j, x, y, cc):
        half = ins[a].shape[0] // 2
        mine = pl.ds(cc * half, half)
        dx, dy = CHIP_RELS[j]
        return pltpu.make_async_remote_copy(src_ref=ins[a].at[mine], dst_ref=outs[a].at[2 * x + y, mine],
                                            send_sem=sems[0].at[a, j], recv_sem=sems[1].at[a, j],
                                            device_id=(x ^ dx, y ^ dy, cc), device_id_type=MESH)

    def d2d(ins, outs, sems, a, j, x, y, cc, half_of):
        half = ins[a].shape[0] // 2
        dx, dy = CHIP_RELS[j]
        landed = outs[a].at[2 * (x ^ dx) + (y ^ dy), pl.ds(half_of * half, half)]
        return pltpu.make_async_remote_copy(src_ref=landed, dst_ref=landed, send_sem=sems[2].at[a, j], recv_sem=sems[3].at[a, j],
                                            device_id=(x, y, 1 - cc), device_id_type=MESH)

    def own(ins, outs, sems, a, x, y):
        return pltpu.make_async_copy(ins[a], outs[a].at[2 * x + y], sems[4].at[a])

    def start(ins, outs, sems):
        x, y, cc = _coords()
        for a in range(n):
            own(ins, outs, sems, a, x, y).start()
            for j in range(3):
                ici(ins, outs, sems, a, j, x, y, cc).start()

    def finish(ins, outs, sems):
        x, y, cc = _coords()
        for a in range(n):
            for j in range(3):
                ici(ins, outs, sems, a, j, x, y, cc).wait_recv()
                d2d(ins, outs, sems, a, j, x, y, cc, cc).start()
        for a in range(n):
            for j in range(3):
                d2d(ins, outs, sems, a, j, x, y, cc, 1 - cc).wait_recv()
        for a in range(n):
            for j in range(3):
                ici(ins, outs, sems, a, j, x, y, cc).wait_send()
                d2d(ins, outs, sems, a, j, x, y, cc, cc).wait_send()
            own(ins, outs, sems, a, x, y).wait()

    dma = pltpu.SemaphoreType.DMA
    return _Comm(shards, [_sds((N_CHIP,) + s.shape, s.dtype) for s in shards],
                 [dma((n, 3)), dma((n, 3)), dma((n, 3)), dma((n, 3)), dma((n,))], start, finish)


def _swap_comm(parts):
    n = len(parts)

    def copy(ins, outs, sems, a):
        x, y, cc = _coords()
        half = ins[a].shape[1] // 2
        return pltpu.make_async_remote_copy(src_ref=ins[a].at[:, pl.ds((1 - cc) * half, half)], dst_ref=outs[a],
                                            send_sem=sems[0].at[a], recv_sem=sems[1].at[a], device_id=(x, y, 1 - cc),
                                            device_id_type=MESH)

    def start(ins, outs, sems):
        for a in range(n):
            copy(ins, outs, sems, a).start()

    def finish(ins, outs, sems):
        for a in range(n):
            copy(ins, outs, sems, a).wait()

    dma = pltpu.SemaphoreType.DMA
    return _Comm(parts, [_sds((p.shape[0], p.shape[1] // 2, p.shape[2]), p.dtype) for p in parts], [dma((n,)), dma((n,))],
                 start, finish)


def _add_half(full, other, cidx, *, name):
    nch, r, c = full.shape
    half = r // 2
    tr = _pick_rows(half)
    nbk = half // tr
    grid_spec = pltpu.PrefetchScalarGridSpec(
        num_scalar_prefetch=1, grid=(nch, nbk),
        in_specs=[pl.BlockSpec((1, tr, c), lambda j, i, cref: (j, cref[0] * nbk + i, 0)),
                  pl.BlockSpec((1, tr, c), lambda j, i, cref: (j, i, 0))],
        out_specs=pl.BlockSpec((1, tr, c), lambda j, i, cref: (j, i, 0)))

    def body(cref, a_ref, b_ref, o_ref):
        o_ref[...] = (a_ref[...] + b_ref[...]).astype(o_ref.dtype)

    return _pcall(body, name=name, out_shape=_sds((nch, half, c), MX), grid_spec=grid_spec)(cidx, full, other)


def _pick_rows(rows, target=512):
    best = None
    for t in range(16, min(rows, target) + 1, 16):
        if rows % t == 0:
            best = t
    return rows if best is None else best


def _exchange_comm(parts):
    n = len(parts)

    def send(ins, outs, sems, a, j, x, y, cc):
        dx, dy = CHIP_RELS[j]
        return pltpu.make_async_remote_copy(src_ref=ins[a].at[2 * (x ^ dx) + (y ^ dy)], dst_ref=outs[a].at[2 * x + y],
                                            send_sem=sems[0].at[a, j], recv_sem=sems[1].at[a, j],
                                            device_id=(x ^ dx, y ^ dy, cc), device_id_type=MESH)

    def landing(ins, outs, sems, a, j, x, y, cc):
        dx, dy = CHIP_RELS[j]
        peer_chip = 2 * (x ^ dx) + (y ^ dy)
        return pltpu.make_async_remote_copy(src_ref=ins[a].at[peer_chip], dst_ref=outs[a].at[peer_chip], send_sem=sems[0].at[a, j],
                                            recv_sem=sems[1].at[a, j], device_id=(x, y, cc), device_id_type=MESH)

    def own(ins, outs, sems, a, x, y):
        return pltpu.make_async_copy(ins[a].at[2 * x + y], outs[a].at[2 * x + y], sems[2].at[a])

    def start(ins, outs, sems):
        x, y, cc = _coords()
        for a in range(n):
            own(ins, outs, sems, a, x, y).start()
            for j in range(3):
                send(ins, outs, sems, a, j, x, y, cc).start()

    def finish(ins, outs, sems):
        x, y, cc = _coords()
        for a in range(n):
            for j in range(3):
                landing(ins, outs, sems, a, j, x, y, cc).wait_recv()
        for a in range(n):
            for j in range(3):
                send(ins, outs, sems, a, j, x, y, cc).wait_send()
            own(ins, outs, sems, a, x, y).wait()

    dma = pltpu.SemaphoreType.DMA
    return _Comm(parts, [_sds(p.shape, p.dtype) for p in parts], [dma((n, 3)), dma((n, 3)), dma((n,))], start, finish)


def _sum_chips(q, cidx, *, name):
    nch, h, c = q.shape
    tr = _pick_rows(h)
    nbk = h // tr
    grid_spec = pltpu.PrefetchScalarGridSpec(
        num_scalar_prefetch=1, grid=(nbk,),
        in_specs=[pl.BlockSpec((nch, tr, c), lambda i, cref: (0, i, 0))],
        out_specs=pl.BlockSpec((tr, c), lambda i, cref: (cref[0] * nbk + i, 0)))

    def body(cref, q_ref, o_ref):
        acc = q_ref[0].astype(F32) + q_ref[1].astype(F32)
        acc = acc + q_ref[2].astype(F32)
        o_ref[...] = acc + q_ref[3].astype(F32)

    return _pcall(body, name=name, out_shape=_sds((2 * h, c), F32), grid_spec=grid_spec)(cidx, q)


def _join_comm(fulls):
    n = len(fulls)

    def half_copy(outs, sems, a, which):
        x, y, cc = _coords()
        h = outs[a].shape[0] // 2
        rows = outs[a].at[pl.ds((cc if which == 0 else 1 - cc) * h, h)]
        return pltpu.make_async_remote_copy(src_ref=rows, dst_ref=rows, send_sem=sems[0].at[a], recv_sem=sems[1].at[a],
                                            device_id=(x, y, 1 - cc), device_id_type=MESH)

    def start(ins, outs, sems):
        for a in range(n):
            half_copy(outs, sems, a, 0).start()

    def finish(ins, outs, sems):
        for a in range(n):
            half_copy(outs, sems, a, 1).wait_recv()
        for a in range(n):
            half_copy(outs, sems, a, 0).wait_send()

    dma = pltpu.SemaphoreType.DMA
    return _Comm(fulls, [_sds(p.shape, p.dtype) for p in fulls], [dma((n,)), dma((n,))], start, finish,
                 aliases={a: a for a in range(n)})


def _ada_prologue(c, w, b, *, name, comm=None):
    nb, dm = c.shape
    n = w.shape[1]

    def body(c_ref, w_ref, b_ref, call_ref, land_ref, cpad, mod_s, g_send, g_recv, m_send, m_recv):
        x, y, cc = _coords()
        me = 4 * x + 2 * y + cc
        chip = 2 * x + y
        cpad[...] = jnp.zeros_like(cpad)
        cpad[0:nb, :] = c_ref[...]
        copies = []
        for kk in range(1, N_DEV):
            peer = (x ^ (kk >> 2), y ^ ((kk >> 1) & 1), cc ^ (kk & 1))
            cp = pltpu.make_async_remote_copy(src_ref=cpad, dst_ref=call_ref.at[me], send_sem=g_send.at[kk - 1],
                                              recv_sem=g_recv.at[kk - 1], device_id=peer, device_id_type=MESH)
            cp.start()
            copies.append(cp)
        call_ref[me] = cpad[...]
        for kk in range(1, N_DEV):
            pltpu.make_async_remote_copy(src_ref=cpad, dst_ref=call_ref.at[me ^ kk], send_sem=g_send.at[kk - 1],
                                         recv_sem=g_recv.at[kk - 1], device_id=(x, y, cc), device_id_type=MESH).wait_recv()
        cv = call_ref[...].reshape(N_DEV * SUBLANE, dm)
        act = (cv * _sigmoid(cv)).astype(MX)
        mod = jnp.dot(act, w_ref[...].astype(MX), preferred_element_type=F32) + b_ref[...]
        mod_s[...] = mod.reshape(N_DEV, SUBLANE, n)
        for j, (dx, dy) in enumerate(CHIP_RELS):
            cp = pltpu.make_async_remote_copy(src_ref=mod_s.at[4 * (x ^ dx) + 2 * (y ^ dy) + cc], dst_ref=land_ref.at[chip],
                                              send_sem=m_send.at[j], recv_sem=m_recv.at[j],
                                              device_id=(x ^ dx, y ^ dy, cc), device_id_type=MESH)
            cp.start()
            copies.append(cp)
        land_ref[chip] = mod_s[me]
        for j, (dx, dy) in enumerate(CHIP_RELS):
            pltpu.make_async_remote_copy(src_ref=mod_s.at[me], dst_ref=land_ref.at[2 * (x ^ dx) + (y ^ dy)], send_sem=m_send.at[j],
                                         recv_sem=m_recv.at[j], device_id=(x, y, cc), device_id_type=MESH).wait_recv()
        for cp in copies:
            cp.wait_send()

    dma = pltpu.SemaphoreType.DMA
    return _pcall(body, name=name, in_specs=[VMEM_SPEC] * 3, out_specs=[VMEM_SPEC] * 2,
                  out_shape=[_sds((N_DEV, SUBLANE, dm), F32), _sds((N_CHIP, SUBLANE, n), F32)],
                  scratch=[pltpu.VMEM((SUBLANE, dm), F32), pltpu.VMEM((N_DEV, SUBLANE, n), F32),
                           dma((N_DEV - 1,)), dma((N_DEV - 1,)), dma((3,)), dma((3,))], comm=comm)(c, w, b)


def _ada_bwd(c_all, dmod_cols, *, name):
    m, kdim = c_all.shape
    n = dmod_cols.shape[1]
    tn = _pick(n, 1152)

    def body(c_ref, d_ref, o_ref):
        cv = c_ref[...]
        act = (cv * _sigmoid(cv)).astype(MX)
        o_ref[...] = lax.dot_general(act, d_ref[...].astype(MX), (((0,), (0,)), ((), ())), preferred_element_type=F32)

    return _pcall(body, name=name, out_shape=_sds((kdim, n), F32), grid=(n // tn,),
                  in_specs=[pl.BlockSpec((m, kdim), lambda j: (0, 0)), pl.BlockSpec((m, tn), lambda j: (0, j))],
                  out_specs=pl.BlockSpec((kdim, tn), lambda j: (0, j)))(c_all, dmod_cols)


SLAB_W = 1024
RED_ROWS = 8


LOSS_LANE = SLAB_W - LANE


def _pack_stats(st1, st2, st3, st_fin, st_q, st_kv, st_prep, d_pool_scale, loss8, *, name):
    nb = st1.shape[0]
    dm_rows = -(-9 * nb // SUBLANE) * SUBLANE

    def body(s1, s2, s3, sf, sq, skv, sp, sps, loss_ref, o_ref):
        o_ref[...] = jnp.zeros_like(o_ref)
        for s in range(nb):
            rows = [s1[s, 0:1, :], s1[s, 1:2, :], s2[s, 3:4, :], s2[s, 0:1, :], s2[s, 1:2, :], s3[s, 3:4, :], s3[s, 0:1, :],
                    s3[s, 1:2, :], sf[s, 0:1, :]]
            for k, row in enumerate(rows):
                o_ref[9 * s + k:9 * s + k + 1, :] = row

        def over_seq(ref, r):
            acc = ref[0, r:r + 1, :]
            for s in range(1, nb):
                acc = acc + ref[s, r:r + 1, :]
            return acc

        o_ref[dm_rows + 0:dm_rows + 1, :] = over_seq(s1, 2)
        o_ref[dm_rows + 1:dm_rows + 2, :] = over_seq(s2, 2)
        o_ref[dm_rows + 2:dm_rows + 3, :] = over_seq(s3, 2)
        o_ref[dm_rows + 3:dm_rows + 4, 0:POOL_W] = over_seq(sps, 0)
        o_ref[dm_rows + 4:dm_rows + 5, 0:QL] = over_seq(sq, 0)
        o_ref[dm_rows + 5:dm_rows + 6, 0:KVL] = over_seq(skv, 0)
        o_ref[dm_rows + 6:dm_rows + 7, 0:LANE] = over_seq(sp, 0)
        o_ref[dm_rows + 7:dm_rows + 8, 0:LANE] = over_seq(sp, 1)
        o_ref[dm_rows + 7:dm_rows + 8, LOSS_LANE:] = loss_ref[0:1, :]

    return _pcall(body, name=name, out_shape=_sds((dm_rows + RED_ROWS, SLAB_W), F32), in_specs=[VMEM_SPEC] * 9,
                  out_specs=VMEM_SPEC)(st1, st2, st3, st_fin, st_q, st_kv, st_prep, d_pool_scale, loss8)


def _small_allreduce(slab, pool, *, name, comm=None):
    rows, w = slab.shape
    dm = rows - RED_ROWS
    prow, pw = pool.shape
    crow = RED_ROWS + 2 * dm

    def body(slab_ref, pool_ref, red_ref, dmod_ref, ptot_ref, sib_slab, sib_pool, chip_slab, chip_pool, land_slab, land_pool,
             a_send, a_recv, b_send, b_recv):
        x, y, cc = _coords()
        chip = 2 * x + y
        sib = (x, y, 1 - cc)
        to_sib = [pltpu.make_async_remote_copy(src_ref=slab_ref, dst_ref=sib_slab, send_sem=a_send.at[0], recv_sem=a_recv.at[0],
                                               device_id=sib, device_id_type=MESH),
                  pltpu.make_async_remote_copy(src_ref=pool_ref, dst_ref=sib_pool, send_sem=a_send.at[1], recv_sem=a_recv.at[1],
                                               device_id=sib, device_id_type=MESH)]
        for cp in to_sib:
            cp.start()
        for cp in to_sib:
            cp.wait()
        mine_dm, theirs_dm = slab_ref[0:dm, :], sib_slab[0:dm, :]
        chip_slab[0:RED_ROWS, :] = slab_ref[dm:, :] + sib_slab[dm:, :]
        chip_slab[RED_ROWS:RED_ROWS + dm, :] = jnp.where(cc == 0, mine_dm, theirs_dm)
        chip_slab[RED_ROWS + dm:, :] = jnp.where(cc == 0, theirs_dm, mine_dm)
        chip_pool[...] = pool_ref[...] + sib_pool[...]

        sends = []
        for j, (dx, dy) in enumerate(CHIP_RELS):
            peer = (x ^ dx, y ^ dy, cc)
            sends.append(pltpu.make_async_remote_copy(src_ref=chip_slab, dst_ref=land_slab.at[chip], send_sem=b_send.at[j, 0],
                                                      recv_sem=b_recv.at[j, 0], device_id=peer, device_id_type=MESH))
            sends.append(pltpu.make_async_remote_copy(src_ref=chip_pool, dst_ref=land_pool.at[chip], send_sem=b_send.at[j, 1],
                                                      recv_sem=b_recv.at[j, 1], device_id=peer, device_id_type=MESH))
        for cp in sends:
            cp.start()
        land_slab[chip] = chip_slab[...]
        land_pool[chip] = chip_pool[...]
        for j, (dx, dy) in enumerate(CHIP_RELS):
            peer_chip = 2 * (x ^ dx) + (y ^ dy)
            pltpu.make_async_remote_copy(src_ref=chip_slab, dst_ref=land_slab.at[peer_chip], send_sem=b_send.at[j, 0],
                                         recv_sem=b_recv.at[j, 0], device_id=(x, y, cc), device_id_type=MESH).wait_recv()
            pltpu.make_async_remote_copy(src_ref=chip_pool, dst_ref=land_pool.at[peer_chip], send_sem=b_send.at[j, 1],
                                         recv_sem=b_recv.at[j, 1], device_id=(x, y, cc), device_id_type=MESH).wait_recv()
        red = land_slab[0, 0:RED_ROWS, :]
        ptot = land_pool[0]
        for ch in range(1, N_CHIP):
            red = red + land_slab[ch, 0:RED_ROWS, :]
            ptot = ptot + land_pool[ch]
        red_ref[...] = red
        ptot_ref[...] = ptot
        for ch in range(N_CHIP):
            dmod_ref[2 * dm * ch:2 * dm * (ch + 1), :] = land_slab[ch, RED_ROWS:, :]
        for cp in sends:
            cp.wait_send()

    dma = pltpu.SemaphoreType.DMA
    return _pcall(body, name=name, in_specs=[VMEM_SPEC, VMEM_SPEC], out_specs=[VMEM_SPEC] * 3,
                  out_shape=[_sds((RED_ROWS, w), F32), _sds((N_DEV * dm, w), F32), _sds((prow, pw), F32)],
                  scratch=[pltpu.VMEM((rows, w), F32), pltpu.VMEM((prow, pw), F32), pltpu.VMEM((crow, w), F32),
                           pltpu.VMEM((prow, pw), F32), pltpu.VMEM((N_CHIP, crow, w), F32), pltpu.VMEM((N_CHIP, prow, pw), F32),
                           dma((2,)), dma((2,)), dma((3, 2)), dma((3, 2))], comm=comm)(slab, pool)


def _adamw_math(w, g, m, v):
    mn = ADAM_B1 * m + (1.0 - ADAM_B1) * g
    vn = ADAM_B2 * v + (1.0 - ADAM_B2) * (g * g)
    bc1 = 1.0 / (1.0 - ADAM_B1 ** ADAM_STEP)
    bc2 = 1.0 / (1.0 - ADAM_B2 ** ADAM_STEP)
    return -ADAM_LR * ((mn * bc1) / (jnp.sqrt(vn * bc2) + ADAM_EPS) + ADAM_WD * w), mn, vn


def _small_update(red, dmod_all, pool_total, nb, params, *, name):
    names = list(SMALL)
    dm = dmod_all.shape[0] // N_DEV

    def grad_of(nm, red_ref, dmod_ref, ptot_ref):
        if nm == "b_ada":
            acc = None
            for d in range(N_DEV):
                for s in range(nb):
                    blk = dmod_ref[d * dm + 9 * s:d * dm + 9 * s + 9, :]
                    acc = blk if acc is None else acc + blk
            return jnp.concatenate([acc[k:k + 1, :] for k in range(9)], axis=1)
        if nm == "pool_grp":
            return ptot_ref[...]
        row, lo, n = {"norm_ffn1": (0, 0, D), "norm_mix": (1, 0, D), "norm_ffn2": (2, 0, D), "pool_scale": (3, 0, POOL_W),
                      "q_a_norm": (4, 0, QL), "kv_a_norm": (5, 0, KVL), "q_norm_nope": (6, 0, NOPE),
                      "q_norm_rope": (6, NOPE, ROPE), "k_norm_nope": (7, 0, NOPE), "k_norm_rope": (7, KR_LANE, ROPE)}[nm]
        return red_ref[row:row + 1, lo:lo + n]

    def body(*refs):
        red_ref, dmod_ref, ptot_ref = refs[:3]
        ins = refs[3:3 + 3 * len(names)]
        outs = refs[3 + 3 * len(names):]
        outs[4 * len(names)][...] = red_ref[RED_ROWS - 1:RED_ROWS, LOSS_LANE:]
        for i, nm in enumerate(names):
            g = grad_of(nm, red_ref, dmod_ref, ptot_ref)
            d, mn, vn = _adamw_math(ins[3 * i][...], g, ins[3 * i + 1][...], ins[3 * i + 2][...])
            outs[4 * i][...] = g
            outs[4 * i + 1][...] = d
            outs[4 * i + 2][...] = mn
            outs[4 * i + 3][...] = vn

    flat_in = [a for nm in names for a in params[nm]]
    out_shape = [_sds(params[nm][0].shape, F32) for nm in names for _ in range(4)] + [_sds((1, LANE), F32)]
    res = _pcall(body, name=name, in_specs=[VMEM_SPEC] * (3 + len(flat_in)), out_specs=[VMEM_SPEC] * len(out_shape),
                 out_shape=out_shape)(red, dmod_all, pool_total, *flat_in)
    return {nm: tuple(res[4 * i:4 * i + 4]) for i, nm in enumerate(names)}, res[-1]


def _adamw(w, g, m, v, *, name, comm=None):
    r, c = w.shape
    tr = _pick_rows(r, 256)
    tc = c if tr < r else _pick(c, 256)
    spec = pl.BlockSpec((tr, tc), lambda i, j: (i, j))
    bc1 = 1.0 / (1.0 - ADAM_B1 ** ADAM_STEP)
    bc2 = 1.0 / (1.0 - ADAM_B2 ** ADAM_STEP)

    def body(w_ref, g_ref, m_ref, v_ref, d_ref, mo_ref, vo_ref):
        gv = g_ref[...]
        mn = ADAM_B1 * m_ref[...] + (1.0 - ADAM_B1) * gv
        vn = ADAM_B2 * v_ref[...] + (1.0 - ADAM_B2) * (gv * gv)
        mo_ref[...] = mn
        vo_ref[...] = vn
        d_ref[...] = -ADAM_LR * ((mn * bc1) / (jnp.sqrt(vn * bc2) + ADAM_EPS) + ADAM_WD * w_ref[...])

    out = _sds((r, c), F32)
    return _pcall(body, name=name, out_shape=[out, out, out], grid=(r // tr, c // tc), in_specs=[spec] * 4, out_specs=[spec] * 3,
                  comm=comm)(w, g, m, v)


BIG = ("w_ffn1_in", "w_ffn1_out", "w_in", "w_pool_proj", "w_q_up", "w_kv_up", "w_mla_proj", "w_out", "w_ffn2_in", "w_ffn2_out")
ROW_SHARDED = ("w_ffn1_out", "w_out", "w_ffn2_out")
KERNEL_NAME = {"w_ffn1_in": "ffn1_in", "w_ffn1_out": "ffn1_out", "w_in": "w_in", "w_pool_proj": "pool_proj", "w_q_up": "q_up",
               "w_kv_up": "kv_up", "w_mla_proj": "mla_proj", "w_out": "w_out", "w_ffn2_in": "ffn2_in", "w_ffn2_out": "ffn2_out"}
WEIGHTS = ("w_ada", "b_ada", "norm_ffn1", "w_ffn1_in", "w_ffn1_out", "norm_mix", "w_in", "pool_grp", "pool_scale", "w_pool_proj",
           "q_a_norm", "w_q_up", "kv_a_norm", "w_kv_up", "q_norm_nope", "q_norm_rope", "k_norm_nope", "k_norm_rope", "w_mla_proj",
           "w_out", "norm_ffn2", "w_ffn2_in", "w_ffn2_out")
SMALL = ("b_ada", "norm_ffn1", "norm_mix", "pool_grp", "pool_scale", "q_a_norm", "kv_a_norm", "q_norm_nope", "q_norm_rope",
         "k_norm_nope", "k_norm_rope", "norm_ffn2")


def _assemble(name, stacked):
    if name in ROW_SHARDED:
        return stacked.reshape(stacked.shape[0] * stacked.shape[1], stacked.shape[2])
    return jnp.transpose(stacked, (1, 0, 2)).reshape(stacked.shape[1], stacked.shape[0] * stacked.shape[2])


def _split(name, full):
    if name in ROW_SHARDED:
        return full.reshape(N_CHIP, full.shape[0] // N_CHIP, full.shape[1])
    return jnp.transpose(full.reshape(full.shape[0], N_CHIP, full.shape[1] // N_CHIP), (1, 0, 2))


GU = ("w_ffn1_in", "w_ffn2_in")
NARROW = ("w_in", "w_q_up")


MIX_SMALL = ("w_out", "w_pool_proj", "w_mla_proj", "w_q_up", "w_kv_up")
RIDES = {
    "ada_prologue": (("gather", ("w_ffn1_in",)),),
    "ffn1_up_a": (("gather", ("w_ffn1_out",)),),
    "ffn1_up_b": (("gather", ("w_in",)),),
    "mix_in": (("gather", MIX_SMALL),),
    "attn_fwd": (("gather", ("w_ffn2_in", "w_ffn2_out")),),
    "d_ffn2_h": (("swap", ("w_ffn2_out", "w_ffn2_in")),),
    "attn_bwd": (("exchange", ("w_ffn2_out", "w_ffn2_in")),),
    "d_mix_in": (("swap", MIX_SMALL),),
    "d_mix_h": (("exchange", MIX_SMALL), ("swap", ("w_in",))),
    "ffn1_dact": (("exchange", ("w_in",)), ("swap", ("w_ffn1_out",))),
    "d_ffn1_in_a": (("exchange", ("w_ffn1_out",)),),
    "d_ffn1_in_b": (("swap", ("w_ffn1_in@0",)),),
    "d_ffn1_h": (("exchange", ("w_ffn1_in@0",)), ("swap", ("w_ffn1_in@1",))),
    "bwd_norm1": (("exchange", ("w_ffn1_in@1",)),),
    "small_allreduce": (("join", tuple(n for n in BIG if n != "w_ffn1_in") + ("w_ffn1_in@0", "w_ffn1_in@1")),),
}


def _kname(n):
    base, _, part = n.partition("@")
    return KERNEL_NAME[base] + ("@" + part if part else "")


def _both(comms):
    if len(comms) == 1:
        return comms[0]
    ins, outs, sems, aliases, spans = [], [], [], {}, []
    for c in comms:
        spans.append((len(ins), len(c.ins), len(outs), len(c.out_shapes), len(sems), len(c.sems)))
        aliases.update({len(ins) + i: len(outs) + o for i, o in c.aliases.items()})
        ins, outs, sems = ins + c.ins, outs + c.out_shapes, sems + c.sems

    def each(which):
        def run(i_, o_, s_):
            for c, (ia, ni, oa, no, sa, ns) in zip(comms, spans):
                getattr(c, which)(i_[ia:ia + ni], o_[oa:oa + no], s_[sa:sa + ns])
        return run

    return _Comm(ins, outs, sems, each("start"), each("finish"), aliases)


class _ExchangePlan:
    def __init__(self, shards, cidx):
        self.shards, self.cidx = shards, cidx
        self.W, self.G, self.parts, self.pre, self.reduced, self.joined = {}, {}, {}, {}, {}, {}

    def grad(self, key, g):
        self.G[key] = g

    def rider(self, name):
        comms = []
        for kind, names in RIDES.get(name, ()):
            if kind == "gather":
                comms.append(_gather_comm([self.shards[n] for n in names]))
            elif kind == "swap":
                for n in names:
                    self.parts[n] = self._stacked(n)
                comms.append(_swap_comm([self.parts[n] for n in names]))
            elif kind == "exchange":
                comms.append(_exchange_comm([self.pre[n] for n in names]))
            else:
                comms.append(_join_comm([self.reduced[n] for n in names]))
        return _both(comms) if comms else None

    def landed(self, name, outs):
        at = 0
        for kind, names in RIDES[name]:
            for n, o in zip(names, outs[at:at + len(names)]):
                if kind == "gather":
                    self.W[KERNEL_NAME[n]] = self._to_kernel(n, o)
                elif kind == "swap":
                    self.pre[n] = _add_half(self.parts[n], o, self.cidx, name="rs_add_" + _kname(n))
                elif kind == "exchange":
                    self.reduced[n] = _sum_chips(o, self.cidx, name="rs_sum_" + _kname(n))
                else:
                    self.joined[n] = o
            at += len(names)

    @staticmethod
    def _to_kernel(n, stacked):
        if n in GU:
            return stacked
        full = _assemble(n, stacked)
        return {"w_in": _w_in_to_kernel, "w_q_up": _q_up_to_kernel, "w_kv_up": _kv_up_to_kernel}.get(n, lambda w: w)(full)

    def _stacked(self, n):
        g = self.G[_kname(n)]
        if n.partition("@")[0] in GU:
            return g
        full = {"w_in": _w_in_from_kernel, "w_q_up": _q_up_from_kernel, "w_kv_up": _kv_up_from_kernel}.get(n, lambda w: w)(g)
        return _split(n, full)


def kernel(x, c, positions, w_ada, b_ada, norm_ffn1, w_ffn1_in, w_ffn1_out, norm_mix, w_in, pool_grp, pool_scale, w_pool_proj, q_a_norm, w_q_up, kv_a_norm, w_kv_up, q_norm_nope, q_norm_rope, k_norm_nope, k_norm_rope, w_mla_proj, w_out, norm_ffn2, w_ffn2_in, w_ffn2_out, loss_target, m_w_ada, m_b_ada, m_norm_ffn1, m_w_ffn1_in, m_w_ffn1_out, m_norm_mix, m_w_in, m_pool_grp, m_pool_scale, m_w_pool_proj, m_q_a_norm, m_w_q_up, m_kv_a_norm, m_w_kv_up, m_q_norm_nope, m_q_norm_rope, m_k_norm_nope, m_k_norm_rope, m_w_mla_proj, m_w_out, m_norm_ffn2, m_w_ffn2_in, m_w_ffn2_out, v_w_ada, v_b_ada, v_norm_ffn1, v_w_ffn1_in, v_w_ffn1_out, v_norm_mix, v_w_in, v_pool_grp, v_pool_scale, v_w_pool_proj, v_q_a_norm, v_w_q_up, v_kv_a_norm, v_w_kv_up, v_q_norm_nope, v_q_norm_rope, v_k_norm_nope, v_k_norm_rope, v_w_mla_proj, v_w_out, v_norm_ffn2, v_w_ffn2_in, v_w_ffn2_out):
    args = dict(locals())
    wts = {n: args[n][0] for n in WEIGHTS}
    mom = {n: args["m_" + n][0] for n in WEIGHTS}
    var = {n: args["v_" + n][0] for n in WEIGHTS}
    nb, seq, dm = x.shape
    tokens = nb * seq
    xi, yi, ci = _coords()
    chip = 2 * xi + yi

    cidx = ci.astype(jnp.int32).reshape(1)
    plan = _ExchangePlan({n: wts[n].astype(MX) for n in BIG}, cidx)
    plan.W["pool_grp"] = wts["pool_grp"].astype(MX)

    ncol = w_ada.shape[2]
    b_cols = lax.dynamic_slice_in_dim(wts["b_ada"].reshape(1, -1), chip * ncol, ncol, axis=1)
    c_slots, mod_slots = _run(plan, _ada_prologue, c, wts["w_ada"], b_cols, name="ada_prologue")
    c_all = c_slots[:, :nb].reshape(N_DEV * nb, dm)
    mod3 = jnp.transpose(mod_slots[:, :nb], (1, 0, 2)).reshape(nb, 9, dm)
    P = {"norm_ffn1": wts["norm_ffn1"].reshape(1, dm), "norm_mix": wts["norm_mix"].reshape(1, dm),
         "norm_ffn2": wts["norm_ffn2"].reshape(1, dm), "pool_scale": wts["pool_scale"].reshape(1, POOL_W),
         "q_a_norm": wts["q_a_norm"].reshape(1, QL), "kv_a_norm": wts["kv_a_norm"].reshape(1, KVL),
         "q_gain": _gain_slab(wts["q_norm_nope"].reshape(1, NOPE), wts["q_norm_rope"].reshape(1, ROPE)),
         "k_gain": _gain_slab(wts["k_norm_nope"].reshape(1, NOPE), wts["k_norm_rope"].reshape(1, ROPE))}
    cos, sin = _rope_tables(positions.reshape(tokens))

    loss8, grad_x, stats, d_pool_grp = _layer_fwd_bwd(x.reshape(tokens, dm), loss_target.reshape(tokens, dm), mod3, cos, sin, plan, P)

    slab = _pack_stats(*stats, loss8, name="pack_stats")
    red, dmod_rows, pool_total = _run(plan, _small_allreduce, slab, d_pool_grp.reshape(POOL_G * LANE, LANE), name="small_allreduce")
    grads = {n: plan.joined[n] for n in BIG if n != "w_ffn1_in"}
    grads["w_ffn1_in"] = jnp.concatenate([plan.joined["w_ffn1_in@0"], plan.joined["w_ffn1_in@1"]], axis=0)
    dm_rows = dmod_rows.shape[0] // N_DEV
    dmod_all = dmod_rows.reshape(N_DEV, dm_rows, SLAB_W)[:, :9 * nb].reshape(N_DEV * nb, 9 * dm)
    dmod_cols = lax.dynamic_slice_in_dim(dmod_all, chip * ncol, ncol, axis=1)
    grads["w_ada"] = _ada_bwd(c_all, dmod_cols, name="ada_bwd")

    delta, new_m, new_v = {}, {}, {}
    as2d = lambda a: a.reshape(POOL_G * LANE, LANE) if a.ndim == 4 else a.reshape(1, -1)
    upd, loss_row = _small_update(red, dmod_rows, pool_total, nb,
                                  {n: tuple(as2d(args[p + n]) for p in ("", "m_", "v_")) for n in SMALL}, name="small_update")
    loss = loss_row[0, 0]
    for n in SMALL:
        grads[n], delta[n], new_m[n], new_v[n] = upd[n]
    for n in ("w_ada",) + BIG:
        if n in NARROW:
            res = _adamw(wts[n].T, grads[n].T, mom[n].T, var[n].T, name="adamw_" + n)
            delta[n], new_m[n], new_v[n] = (r.T for r in res)
        else:
            delta[n], new_m[n], new_v[n] = _adamw(wts[n], grads[n], mom[n], var[n], name="adamw_" + n)

    def lead(a, n):
        return a.reshape((1,) + wts[n].shape)

    return (loss, grad_x.reshape(nb, seq, dm), *[lead(grads[n], n) for n in WEIGHTS], *[lead(delta[n], n) for n in WEIGHTS],
            *[lead(new_m[n], n) for n in WEIGHTS], *[lead(new_v[n], n) for n in WEIGHTS])
```

```python
import functools
import math

import jax
import jax.numpy as jnp
from jax import lax
from jax.experimental import pallas as pl
from jax.experimental.pallas import tpu as pltpu

F32 = jnp.float32
MX = jnp.bfloat16

D = 1024
DFF = 2816
NH = 8
POOL_W = 512
POOL_G = 4
QL = 384
KVL = 256
ROPE = 32
NOPE = 64
LANE = 128
SUBLANE = 8
EPS = 1e-6
ATTN_SCALE = 1.0 / math.sqrt(96.0)
NEG = -1e30

Z_UP, Z_QL, Z_KV, Z_KR, Z_GP, Z_GM, Z_W = 0, 512, 896, 1152, 1280, 2304, 3328
KR_LANE = 64
LAT_W = 1280

ADAM_LR, ADAM_B1, ADAM_B2, ADAM_EPS, ADAM_WD, ADAM_STEP = 0.001, 0.9, 0.999, 1e-08, 0.01, 10

VMEM_LIMIT = 48 * 1024 * 1024
VMEM_FLOOR = 16 * 1024 * 1024
MESH = pl.DeviceIdType.MESH
N_DEV = 8
N_CHIP = 4


class _Comm:
    def __init__(self, ins, out_shapes, sems, start, finish, aliases=None):
        self.ins, self.out_shapes, self.sems = list(ins), list(out_shapes), list(sems)
        self.start, self.finish = start, finish
        self.aliases = aliases or {}


def _vmem_limit(args, in_specs, outs, out_specs, scratch):
    def block_bytes(spec, shape, dtype):
        if spec.memory_space is pl.ANY:
            return 0
        dims = shape if spec.block_shape is None else [d for d in spec.block_shape if d is not None]
        return math.prod(dims) * jnp.dtype(dtype).itemsize

    blocks = sum(block_bytes(s, a.shape, a.dtype) for s, a in zip(in_specs, args))
    blocks += sum(block_bytes(s, o.shape, o.dtype) for s, o in zip(out_specs, outs))
    held = sum(math.prod(s.shape) * jnp.dtype(s.dtype).itemsize for s in scratch if getattr(s, "memory_space", None) == pltpu.VMEM)
    return int(min(VMEM_LIMIT, max(VMEM_FLOOR, 3 * blocks + held + VMEM_FLOOR // 2)))


def _pcall(body, *, name, out_shape, grid=(), in_specs=None, out_specs=None, scratch=(), grid_spec=None, aliases=None,
           comm=None):
    kw = dict(name=name)
    single = not isinstance(out_shape, (list, tuple))
    outs = [out_shape] if single else list(out_shape)
    ospecs = [out_specs] if single else list(out_specs or [])
    if comm is None:
        if aliases:
            kw["input_output_aliases"] = aliases
        if grid_spec is not None:
            return pl.pallas_call(body, grid_spec=grid_spec, out_shape=out_shape,
                                  compiler_params=pltpu.CompilerParams(vmem_limit_bytes=VMEM_LIMIT), **kw)

        def run_plain(*args):
            limit = _vmem_limit(args, in_specs, outs, ospecs, scratch)
            return pl.pallas_call(body, grid=grid, in_specs=in_specs, out_specs=out_specs, scratch_shapes=scratch,
                                  out_shape=out_shape, compiler_params=pltpu.CompilerParams(vmem_limit_bytes=limit), **kw)(*args)

        return run_plain
    n_in, n_out, n_ci, n_co, n_scr = len(in_specs), len(outs), len(comm.ins), len(comm.out_shapes), len(scratch)
    io = dict(aliases or {})
    io.update({n_in + i: n_out + o for i, o in comm.aliases.items()})

    def riding(*refs):
        ins, cins = refs[:n_in], refs[n_in:n_in + n_ci]
        at = n_in + n_ci
        os_, couts = refs[at:at + n_out], refs[at + n_out:at + n_out + n_co]
        at += n_out + n_co
        scr, csems = refs[at:at + n_scr], refs[at + n_scr:]
        if grid:
            first = functools.reduce(jnp.logical_and, [pl.program_id(d) == 0 for d in range(len(grid))])
            last = functools.reduce(jnp.logical_and, [pl.program_id(d) == grid[d] - 1 for d in range(len(grid))])
            pl.when(first)(lambda: comm.start(cins, couts, csems))
            body(*ins, *os_, *scr)
            pl.when(last)(lambda: comm.finish(cins, couts, csems))
        else:
            comm.start(cins, couts, csems)
            body(*ins, *os_, *scr)
            comm.finish(cins, couts, csems)

    def run(*args):
        limit = _vmem_limit(args, in_specs, outs, ospecs, scratch)
        call = pl.pallas_call(riding, grid=grid, in_specs=list(in_specs) + [HBM_SPEC] * n_ci, out_specs=ospecs + [HBM_SPEC] * n_co,
                              out_shape=outs + comm.out_shapes, scratch_shapes=list(scratch) + comm.sems,
                              input_output_aliases=io, compiler_params=pltpu.CompilerParams(vmem_limit_bytes=limit), **kw)
        res = call(*args, *comm.ins)
        main = list(res[:n_out])
        return (main[0] if single else main), list(res[n_out:])

    return run


def _pick(dim, target):
    best = None
    for t in range(LANE, min(dim, target) + 1, LANE):
        if dim % t == 0:
            best = t
    return dim if best is None else best


def _sds(shape, dtype):
    return jax.ShapeDtypeStruct(shape, dtype)


def _dw(a, g, *, name, tm=512, tn=1024, comm=None):
    return _mm(a, g, mode="tn", out_dtype=F32, name=name, tm=tm, tn=tn, tk=a.shape[0], n_outer=True, comm=comm)


def _mm(a, b, *, mode, out_dtype, name, tm=1024, tn=1024, tk=4096, n_outer=False, comm=None):
    if mode == "nn":
        (M, K), (K2, N) = a.shape, b.shape
    elif mode == "nt":
        (M, K), (N, K2) = a.shape, b.shape
    else:
        (K, M), (K2, N) = a.shape, b.shape
    assert K == K2, (name, a.shape, b.shape)
    tm, tn, tk = _pick(M, tm), _pick(N, tn), _pick(K, tk)
    nk = K // tk
    if n_outer:
        ij = lambda g0, g1: (g1, g0)
        grid = (N // tn, M // tm, nk)
    else:
        ij = lambda g0, g1: (g0, g1)
        grid = (M // tm, N // tn, nk)
    if mode == "tn":
        a_spec = pl.BlockSpec((tk, tm), lambda g0, g1, k: (k, ij(g0, g1)[0]))
    else:
        a_spec = pl.BlockSpec((tm, tk), lambda g0, g1, k: (ij(g0, g1)[0], k))
    if mode == "nt":
        b_spec = pl.BlockSpec((tn, tk), lambda g0, g1, k: (ij(g0, g1)[1], k))
    else:
        b_spec = pl.BlockSpec((tk, tn), lambda g0, g1, k: (k, ij(g0, g1)[1]))
    o_spec = pl.BlockSpec((tm, tn), lambda g0, g1, k: ij(g0, g1))
    dn = {"nn": (((1,), (0,)), ((), ())), "nt": (((1,), (1,)), ((), ())), "tn": (((0,), (0,)), ((), ()))}[mode]

    def dot(a_ref, b_ref):
        return lax.dot_general(a_ref[...].astype(MX), b_ref[...].astype(MX), dn, preferred_element_type=F32)

    def body_one(a_ref, b_ref, o_ref):
        o_ref[...] = dot(a_ref, b_ref).astype(o_ref.dtype)

    def body_acc(a_ref, b_ref, o_ref, acc_ref):
        k = pl.program_id(2)
        part = dot(a_ref, b_ref)

        @pl.when(k == 0)
        def _():
            acc_ref[...] = part

        @pl.when(k > 0)
        def _():
            acc_ref[...] += part

        @pl.when(k == nk - 1)
        def _():
            o_ref[...] = acc_ref[...].astype(o_ref.dtype)

    return _pcall(body_one if nk == 1 else body_acc, name=name, out_shape=_sds((M, N), out_dtype), grid=grid,
                  in_specs=[a_spec, b_spec], out_specs=o_spec, scratch=[] if nk == 1 else [pltpu.VMEM((tm, tn), F32)],
                  comm=comm)(a, b)


def _gu_shard(q):
    return (q % 2) * 2 + q // 2


def _ffn_up(h, w_st, *, name, tm=512, comm=None):
    T, dm = h.shape
    hw = w_st.shape[2]
    tm = _pick(T, tm)

    def body(h_ref, wg_ref, wu_ref, gu_ref, a_ref):
        hv = h_ref[...]
        g = jnp.dot(hv, wg_ref[0], preferred_element_type=F32)
        u = jnp.dot(hv, wu_ref[0], preferred_element_type=F32)
        gu_ref[:, :hw] = g.astype(gu_ref.dtype)
        gu_ref[:, hw:] = u.astype(gu_ref.dtype)
        a_ref[...] = (g * _sigmoid(g) * u).astype(a_ref.dtype)

    return _pcall(body, name=name, grid=(T // tm, 2),
                  in_specs=[pl.BlockSpec((tm, dm), lambda i, j: (i, 0)), pl.BlockSpec((1, dm, hw), lambda i, j: (j, 0, 0)),
                            pl.BlockSpec((1, dm, hw), lambda i, j: (2 + j, 0, 0))],
                  out_specs=[pl.BlockSpec((tm, 2 * hw), lambda i, j: (i, j)), pl.BlockSpec((tm, hw), lambda i, j: (i, j))],
                  out_shape=[_sds((T, 4 * hw), MX), _sds((T, 2 * hw), MX)], comm=comm)(h, w_st, w_st)


def _ffn_up_first(x, mod3, gain, w_st, *, sub, name, tm=512, comm=None):
    T, dm = x.shape
    hw = w_st.shape[2]
    tm = _pick(T, tm)
    tps = (T // mod3.shape[0]) // tm

    def body(x_ref, mod_ref, n_ref, wg_ref, wu_ref, h_ref, gu_ref, a_ref):
        xv = x_ref[...]
        xn = xv * _rsq(xv) * n_ref[...]
        hv = (xn * (1.0 + mod_ref[0, 3 * sub + 1:3 * sub + 2, :]) + mod_ref[0, 3 * sub:3 * sub + 1, :]).astype(h_ref.dtype)
        h_ref[...] = hv
        g = jnp.dot(hv, wg_ref[0], preferred_element_type=F32)
        u = jnp.dot(hv, wu_ref[0], preferred_element_type=F32)
        gu_ref[:, :hw] = g.astype(gu_ref.dtype)
        gu_ref[:, hw:] = u.astype(gu_ref.dtype)
        a_ref[...] = (g * _sigmoid(g) * u).astype(a_ref.dtype)

    return _pcall(body, name=name, grid=(T // tm,),
                  in_specs=[_row_spec(tm, dm), pl.BlockSpec((1, 9, dm), lambda i: (i // tps, 0, 0)),
                            pl.BlockSpec((1, dm), lambda i: (0, 0)), pl.BlockSpec((1, dm, hw), lambda i: (0, 0, 0)),
                            pl.BlockSpec((1, dm, hw), lambda i: (2, 0, 0))],
                  out_specs=[_row_spec(tm, dm), pl.BlockSpec((tm, 2 * hw), lambda i: (i, 0)), pl.BlockSpec((tm, hw), lambda i: (i, 0))],
                  out_shape=[_sds((T, dm), MX), _sds((T, 4 * hw), MX), _sds((T, 2 * hw), MX)],
                  comm=comm)(x, mod3, gain, w_st, w_st)


def _ffn_up_second(h, w_st, gu, a, *, name, tm=512, comm=None):
    T, dm = h.shape
    hw = w_st.shape[2]
    tm = _pick(T, tm)

    def body(h_ref, wg_ref, wu_ref, gu_in, a_in, gu_ref, a_ref):
        hv = h_ref[...]
        g = jnp.dot(hv, wg_ref[0], preferred_element_type=F32)
        u = jnp.dot(hv, wu_ref[0], preferred_element_type=F32)
        gu_ref[:, :hw] = g.astype(gu_ref.dtype)
        gu_ref[:, hw:] = u.astype(gu_ref.dtype)
        a_ref[...] = (g * _sigmoid(g) * u).astype(a_ref.dtype)

    return _pcall(body, name=name, grid=(T // tm,),
                  in_specs=[_row_spec(tm, dm), pl.BlockSpec((1, dm, hw), lambda i: (1, 0, 0)),
                            pl.BlockSpec((1, dm, hw), lambda i: (3, 0, 0)), HBM_SPEC, HBM_SPEC],
                  out_specs=[pl.BlockSpec((tm, 2 * hw), lambda i: (i, 1)), pl.BlockSpec((tm, hw), lambda i: (i, 1))],
                  out_shape=[_sds(gu.shape, gu.dtype), _sds(a.shape, a.dtype)], aliases={3: 0, 4: 1},
                  comm=comm)(h, w_st, w_st, gu, a)


def _ffn_dact(df, gu, w_out, *, name, tm=512, comm=None):
    T, dm = df.shape
    hw = gu.shape[1] // 4
    tm = _pick(T, tm)

    def body(df_ref, gu_ref, wo_ref, dgu_ref):
        da = lax.dot_general(df_ref[...], wo_ref[...], (((1,), (1,)), ((), ())), preferred_element_type=F32)
        g = gu_ref[:, :hw].astype(F32)
        u = gu_ref[:, hw:].astype(F32)
        s = _sigmoid(g)
        dgu_ref[:, :hw] = (da * u * (s * (1.0 + g * (1.0 - s)))).astype(dgu_ref.dtype)
        dgu_ref[:, hw:] = (da * (g * s)).astype(dgu_ref.dtype)

    return _pcall(body, name=name, grid=(T // tm, 2),
                  in_specs=[pl.BlockSpec((tm, dm), lambda i, j: (i, 0)), pl.BlockSpec((tm, 2 * hw), lambda i, j: (i, j)),
                            pl.BlockSpec((hw, dm), lambda i, j: (j, 0))],
                  out_specs=pl.BlockSpec((tm, 2 * hw), lambda i, j: (i, j)), out_shape=_sds(gu.shape, MX),
                  comm=comm)(df, gu, w_out)


def _ffn_dh(dgu, w_st, *, name, tm=1024, comm=None):
    T = dgu.shape[0]
    _, dm, hw = w_st.shape
    tm = _pick(T, tm)

    def body(d_ref, w_ref, o_ref, acc_ref):
        q = pl.program_id(1)
        part = lax.dot_general(d_ref[...], w_ref[0], (((1,), (1,)), ((), ())), preferred_element_type=F32)

        @pl.when(q == 0)
        def _():
            acc_ref[...] = part

        @pl.when(jnp.logical_and(q > 0, q < 3))
        def _():
            acc_ref[...] += part

        @pl.when(q == 3)
        def _():
            o_ref[...] = acc_ref[...] + part

    return _pcall(body, name=name, grid=(T // tm, 4),
                  in_specs=[pl.BlockSpec((tm, hw), lambda i, q: (i, q)), pl.BlockSpec((1, dm, hw), lambda i, q: (_gu_shard(q), 0, 0))],
                  out_specs=pl.BlockSpec((tm, dm), lambda i, q: (i, 0)), out_shape=_sds((T, dm), F32),
                  scratch=[pltpu.VMEM((tm, dm), F32)], comm=comm)(dgu, w_st)


def _ffn_dw_in(h, dgu, *, name, rows=None, tm=512, comm=None):
    T, dm = h.shape
    hw = dgu.shape[1] // 4
    first, count = rows if rows is not None else (0, dm)
    tm = _pick(count, tm)
    skip = first // tm

    def body(h_ref, d_ref, o_ref):
        o_ref[0] = lax.dot_general(h_ref[...], d_ref[...], (((0,), (0,)), ((), ())), preferred_element_type=F32)

    return _pcall(body, name=name, grid=(4, count // tm),
                  in_specs=[pl.BlockSpec((T, tm), lambda q, i: (0, skip + i)), pl.BlockSpec((T, hw), lambda q, i: (0, q))],
                  out_specs=pl.BlockSpec((1, tm, hw), lambda q, i: (_gu_shard(q), i, 0)),
                  out_shape=_sds((4, count, hw), F32), comm=comm)(h, dgu)


def _rsq(x):
    return lax.rsqrt(jnp.mean(x * x, axis=-1, keepdims=True) + EPS)


def _sigmoid(x):
    return 1.0 / (1.0 + jnp.exp(-x))


def _row_spec(tm, w):
    return pl.BlockSpec((tm, w), lambda i: (i, 0))


def _mm_resid_norm(a, w, x_prev, mod3, gain, *, sub, coef, name, tm=512, comm=None):
    T, k = a.shape
    dm = w.shape[1]
    tm = _pick(T, tm)
    tps = (T // mod3.shape[0]) // tm

    def body(a_ref, w_ref, x_ref, mod_ref, n_ref, f_ref, xo_ref, h_ref):
        f = jnp.dot(a_ref[...], w_ref[...], preferred_element_type=F32)
        f_ref[...] = f
        x = x_ref[...] + coef * mod_ref[0, 3 * sub - 1:3 * sub, :] * f
        xo_ref[...] = x
        xn = x * _rsq(x) * n_ref[...]
        h_ref[...] = (xn * (1.0 + mod_ref[0, 3 * sub + 1:3 * sub + 2, :]) + mod_ref[0, 3 * sub:3 * sub + 1, :]).astype(h_ref.dtype)

    row = _row_spec(tm, dm)
    return _pcall(body, name=name, grid=(T // tm,),
                  in_specs=[_row_spec(tm, k), pl.BlockSpec((k, dm), lambda i: (0, 0)), row,
                            pl.BlockSpec((1, 9, dm), lambda i: (i // tps, 0, 0)), pl.BlockSpec((1, dm), lambda i: (0, 0))],
                  out_specs=[row, row, row], out_shape=[_sds((T, dm), F32), _sds((T, dm), F32), _sds((T, dm), MX)],
                  comm=comm)(a, w, x_prev, mod3, gain)


def _mm_loss(a, w, x2, tgt, mod3, *, name, tm=512):
    T, k = a.shape
    dm = w.shape[1]
    tm = _pick(T, tm)
    tps = (T // mod3.shape[0]) // tm
    mod_spec = pl.BlockSpec((1, 9, dm), lambda i: (i // tps, 0, 0))
    row = _row_spec(tm, dm)
    stat_spec = pl.BlockSpec((1, SUBLANE, dm), lambda i: (i // tps, 0, 0))
    loss_spec = pl.BlockSpec((SUBLANE, LANE), lambda i: (0, 0))

    def body(a_ref, w_ref, x_ref, t_ref, mod_ref, dy_ref, df_ref, st_ref, loss_ref):
        i = pl.program_id(0)
        g = mod_ref[0, 8:9, :]
        f = jnp.dot(a_ref[...], w_ref[...], preferred_element_type=F32)
        err = x_ref[...] + 0.5 * g * f - t_ref[...]
        dy = err * (1.0 / dm)
        dy_ref[...] = dy
        df_ref[...] = (0.5 * g * dy).astype(df_ref.dtype)
        dgate = jnp.sum(0.5 * dy * f, axis=0, keepdims=True)
        part = 0.5 * jnp.sum(jnp.sum(err * err, axis=0, keepdims=True), axis=1, keepdims=True) * (1.0 / dm)

        @pl.when(i % tps == 0)
        def _():
            st_ref[...] = jnp.zeros_like(st_ref)

        @pl.when(i == 0)
        def _():
            loss_ref[...] = jnp.zeros_like(loss_ref)

        st_ref[0, 0:1, :] += dgate
        loss_ref[...] += jnp.broadcast_to(part, loss_ref.shape)

    return _pcall(body, name=name, grid=(T // tm,),
                  in_specs=[_row_spec(tm, k), pl.BlockSpec((k, dm), lambda i: (0, 0)), row, row, mod_spec],
                  out_specs=[row, row, stat_spec, loss_spec],
                  out_shape=[_sds((T, dm), F32), _sds((T, dm), MX), _sds((mod3.shape[0], SUBLANE, dm), F32),
                             _sds((SUBLANE, LANE), F32)])(a, w, x2, tgt, mod3)


def _norm_bwd_tail(dhv, x_ref, dxi_ref, f_ref, mod_ref, n_ref, dx_ref, df_ref, st_ref, *, first, sub, coef):
    x = x_ref[...]
    r = _rsq(x)
    xhat = x * r
    n = n_ref[...]
    d_shift = jnp.sum(dhv, axis=0, keepdims=True)
    d_scale = jnp.sum(dhv * (xhat * n), axis=0, keepdims=True)
    dxn = dhv * (1.0 + mod_ref[0, 3 * sub + 1:3 * sub + 2, :])
    d_gain = jnp.sum(dxn * xhat, axis=0, keepdims=True)
    dxhat = dxn * n
    dx = dxi_ref[...] + r * (dxhat - xhat * jnp.mean(dxhat * xhat, axis=-1, keepdims=True))
    dx_ref[...] = dx

    @pl.when(first)
    def _():
        st_ref[...] = jnp.zeros_like(st_ref)

    st_ref[0, 0:1, :] += d_shift
    st_ref[0, 1:2, :] += d_scale
    st_ref[0, 2:3, :] += d_gain
    if f_ref is not None:
        st_ref[0, 3:4, :] += jnp.sum(coef * dx * f_ref[...], axis=0, keepdims=True)
        df_ref[...] = (coef * mod_ref[0, 3 * sub - 1:3 * sub, :] * dx).astype(df_ref.dtype)


def _ffn_dh_norm(dgu, w_st, x_cur, dx_in, f_prev, mod3, gain, *, sub, coef, name, tm=512, comm=None):
    T = dgu.shape[0]
    _, dm, hw = w_st.shape
    tm = _pick(T, tm)
    nb = mod3.shape[0]
    tps = (T // nb) // tm

    def body(d_ref, w_ref, x_ref, dxi_ref, f_ref, mod_ref, n_ref, dx_ref, df_ref, st_ref, acc_ref):
        i = pl.program_id(0)
        q = pl.program_id(1)
        part = lax.dot_general(d_ref[...], w_ref[0], (((1,), (1,)), ((), ())), preferred_element_type=F32)

        @pl.when(q == 0)
        def _():
            acc_ref[...] = part

        @pl.when(q > 0)
        def _():
            acc_ref[...] += part

        @pl.when(q == 3)
        def _():
            _norm_bwd_tail(acc_ref[...], x_ref, dxi_ref, f_ref, mod_ref, n_ref, dx_ref, df_ref, st_ref,
                           first=i % tps == 0, sub=sub, coef=coef)

    row = pl.BlockSpec((tm, dm), lambda i, q: (i, 0))
    return _pcall(body, name=name, grid=(T // tm, 4),
                  in_specs=[pl.BlockSpec((tm, hw), lambda i, q: (i, q)), pl.BlockSpec((1, dm, hw), lambda i, q: (_gu_shard(q), 0, 0)),
                            row, row, row, pl.BlockSpec((1, 9, dm), lambda i, q: (i // tps, 0, 0)),
                            pl.BlockSpec((1, dm), lambda i, q: (0, 0))],
                  out_specs=[row, row, pl.BlockSpec((1, SUBLANE, dm), lambda i, q: (i // tps, 0, 0))],
                  out_shape=[_sds((T, dm), F32), _sds((T, dm), MX), _sds((nb, SUBLANE, dm), F32)],
                  scratch=[pltpu.VMEM((tm, dm), F32)], comm=comm)(dgu, w_st, x_cur, dx_in, f_prev, mod3, gain)


def _mm_nt_norm(a, b, x_cur, dx_in, f_prev, mod3, gain, *, sub, coef, name, tm=512, comm=None):
    T, k = a.shape
    dm = b.shape[0]
    tm = _pick(T, tm)
    nb = mod3.shape[0]
    tps = (T // nb) // tm

    def body(a_ref, b_ref, x_ref, dxi_ref, f_ref, mod_ref, n_ref, dx_ref, df_ref, st_ref):
        dh = lax.dot_general(a_ref[...], b_ref[...], (((1,), (1,)), ((), ())), preferred_element_type=F32)
        _norm_bwd_tail(dh, x_ref, dxi_ref, f_ref, mod_ref, n_ref, dx_ref, df_ref, st_ref,
                       first=pl.program_id(0) % tps == 0, sub=sub, coef=coef)

    row = _row_spec(tm, dm)
    return _pcall(body, name=name, grid=(T // tm,),
                  in_specs=[_row_spec(tm, k), pl.BlockSpec((dm, k), lambda i: (0, 0)), row, row, row,
                            pl.BlockSpec((1, 9, dm), lambda i: (i // tps, 0, 0)), pl.BlockSpec((1, dm), lambda i: (0, 0))],
                  out_specs=[row, row, pl.BlockSpec((1, SUBLANE, dm), lambda i: (i // tps, 0, 0))],
                  out_shape=[_sds((T, dm), F32), _sds((T, dm), MX), _sds((nb, SUBLANE, dm), F32)],
                  comm=comm)(a, b, x_cur, dx_in, f_prev, mod3, gain)


def _bwd_block(x_cur, dh, dx_in, f_prev, mod3, gain, *, sub, coef, name, tm=256, comm=None):
    T, dm = x_cur.shape
    nb = mod3.shape[0]
    tps = (T // nb) // tm
    has_f = f_prev is not None
    mod_spec = pl.BlockSpec((1, 9, dm), lambda i: (i // tps, 0, 0))
    vec_spec = pl.BlockSpec((1, dm), lambda i: (0, 0))
    stat_spec = pl.BlockSpec((1, SUBLANE, dm), lambda i: (i // tps, 0, 0))
    row = _row_spec(tm, dm)

    def body(*refs):
        if has_f:
            x_ref, dh_ref, dxi_ref, f_ref, mod_ref, n_ref, dx_ref, df_ref, st_ref = refs
        else:
            x_ref, dh_ref, dxi_ref, mod_ref, n_ref, dx_ref, st_ref = refs
            f_ref = df_ref = None
        _norm_bwd_tail(dh_ref[...], x_ref, dxi_ref, f_ref, mod_ref, n_ref, dx_ref, df_ref, st_ref,
                       first=pl.program_id(0) % tps == 0, sub=sub, coef=coef)

    st_shape = _sds((nb, SUBLANE, dm), F32)
    if has_f:
        return _pcall(body, name=name, grid=(T // tm,), in_specs=[row, row, row, row, mod_spec, vec_spec],
                      out_specs=[row, row, stat_spec],
                      out_shape=[_sds((T, dm), F32), _sds((T, dm), MX), st_shape], comm=comm)(x_cur, dh, dx_in, f_prev, mod3, gain)
    return _pcall(body, name=name, grid=(T // tm,), in_specs=[row, row, row, mod_spec, vec_spec],
                  out_specs=[row, stat_spec], out_shape=[_sds((T, dm), F32), st_shape], comm=comm)(x_cur, dh, dx_in, mod3, gain)


def _mix_out_fwd(z, br_pool, attn, w_mla, w_out, x_prev, mod3, gain, *, sub, name, tm=512):
    T = z.shape[0]
    dm = w_out.shape[1]
    tm = _pick(T, tm)
    tps = (T // mod3.shape[0]) // tm
    whole = lambda a: pl.BlockSpec(a.shape, lambda i: (0, 0))

    def body(z_ref, bp_ref, at_ref, wm_ref, wo_ref, x_ref, mod_ref, n_ref, bm_ref, mg_ref, mo_ref, xo_ref, h_ref):
        bm = jnp.dot(at_ref[...], wm_ref[...], preferred_element_type=F32).astype(MX)
        bm_ref[...] = bm
        gp = z_ref[:, Z_GP:Z_GP + D].astype(F32)
        gm = z_ref[:, Z_GM:Z_GM + D].astype(F32)
        merged = (_sigmoid(gp) * bp_ref[...] + _sigmoid(gm) * bm).astype(MX)
        mg_ref[...] = merged
        mo = jnp.dot(merged, wo_ref[...], preferred_element_type=F32)
        mo_ref[...] = mo
        x = x_ref[...] + mod_ref[0, 3 * sub - 1:3 * sub, :] * mo
        xo_ref[...] = x
        xn = x * _rsq(x) * n_ref[...]
        h_ref[...] = (xn * (1.0 + mod_ref[0, 3 * sub + 1:3 * sub + 2, :]) + mod_ref[0, 3 * sub:3 * sub + 1, :]).astype(h_ref.dtype)

    row = _row_spec(tm, dm)
    return _pcall(body, name=name, grid=(T // tm,),
                  in_specs=[_row_spec(tm, Z_W), row, _row_spec(tm, attn.shape[1]), whole(w_mla), whole(w_out), row,
                            pl.BlockSpec((1, 9, dm), lambda i: (i // tps, 0, 0)), pl.BlockSpec((1, dm), lambda i: (0, 0))],
                  out_specs=[row, row, row, row, row],
                  out_shape=[_sds((T, dm), MX), _sds((T, dm), MX), _sds((T, dm), F32), _sds((T, dm), F32), _sds((T, dm), MX)],
                  )(z, br_pool, attn, w_mla, w_out, x_prev, mod3, gain)


def _mix_out_bwd(dmo, w_out, z, br_pool, br_mla, w_mla, *, name, tm=512):
    T = z.shape[0]
    tm = _pick(T, tm)
    whole = lambda a: pl.BlockSpec(a.shape, lambda i: (0, 0))
    nt = (((1,), (1,)), ((), ()))

    def body(dmo_ref, wo_ref, z_ref, bp_ref, bm_ref, wm_ref, dbp_ref, dbm_ref, dg_ref, dat_ref):
        dm = lax.dot_general(dmo_ref[...], wo_ref[...], nt, preferred_element_type=F32)
        sp = _sigmoid(z_ref[:, Z_GP:Z_GP + D].astype(F32))
        sm = _sigmoid(z_ref[:, Z_GM:Z_GM + D].astype(F32))
        dbp_ref[...] = (dm * sp).astype(dbp_ref.dtype)
        dbm = (dm * sm).astype(MX)
        dbm_ref[...] = dbm
        dg_ref[:, :D] = (dm * bp_ref[...].astype(F32) * sp * (1.0 - sp)).astype(dg_ref.dtype)
        dg_ref[:, D:] = (dm * bm_ref[...].astype(F32) * sm * (1.0 - sm)).astype(dg_ref.dtype)
        dat_ref[...] = lax.dot_general(dbm, wm_ref[...], nt, preferred_element_type=F32)

    row = _row_spec(tm, D)
    return _pcall(body, name=name, grid=(T // tm,),
                  in_specs=[row, whole(w_out), _row_spec(tm, Z_W), row, row, whole(w_mla)],
                  out_specs=[row, row, _row_spec(tm, 2 * D), _row_spec(tm, w_mla.shape[0])],
                  out_shape=[_sds((T, D), MX), _sds((T, D), MX), _sds((T, 2 * D), MX), _sds((T, w_mla.shape[0]), F32)],
                  )(dmo, w_out, z, br_pool, br_mla, w_mla)


def _shift_down(x, k, row):
    return jnp.where(row >= k, pltpu.roll(x, k, 0), 0.0)


def _shift_up(x, k, row, n):
    return jnp.where(row < n - k, pltpu.roll(x, n - k, 0), 0.0)


def _pool_fwd(z, pool_grp, pool_scale, *, nb, name):
    T = z.shape[0]
    S = T // nb
    blk = pl.BlockSpec((S, LANE), lambda b, g: (b, g))

    def body(u_ref, w_ref, s_ref, pooled_ref, mixed_ref, scaled_ref):
        g = pl.program_id(1)
        u = u_ref[...].astype(F32)
        row = lax.broadcasted_iota(jnp.int32, u.shape, 0)
        s2 = u + _shift_down(u, 1, row)
        s4 = s2 + _shift_down(s2, 2, row)
        s8 = s4 + _shift_down(s4, 4, row)
        s16 = s8 + _shift_down(s8, 8, row)
        win = jnp.where(g == 0, s2, jnp.where(g == 1, s4, jnp.where(g == 2, s8, s16)))
        width = lax.shift_left(jnp.int32(2), g)
        cnt = jnp.minimum(row + 1, width).astype(F32)
        pooled = (win / cnt - u).astype(MX)
        pooled_ref[...] = pooled
        mixed = jnp.dot(pooled, w_ref[0], preferred_element_type=F32)
        mixed_ref[...] = mixed
        scaled_ref[...] = (mixed * s_ref[...]).astype(scaled_ref.dtype)

    return _pcall(body, name=name, grid=(nb, POOL_G),
                  in_specs=[blk, pl.BlockSpec((1, LANE, LANE), lambda b, g: (g, 0, 0)),
                            pl.BlockSpec((1, LANE), lambda b, g: (0, g))],
                  out_specs=[blk, blk, blk],
                  out_shape=[_sds((T, POOL_W), MX), _sds((T, POOL_W), F32), _sds((T, POOL_W), MX)])(z, pool_grp, pool_scale)


def _pool_bwd(dscaled, mixed, pooled, pool_grp, pool_scale, *, nb, name):
    T = dscaled.shape[0]
    S = T // nb
    blk = pl.BlockSpec((S, LANE), lambda g, b: (b, g))

    def body(ds_ref, mixed_ref, pooled_ref, w_ref, s_ref, du_ref, dw_ref, dsc_ref):
        g = pl.program_id(0)
        b = pl.program_id(1)
        ds = ds_ref[...]
        dsc_ref[0] = jnp.sum(ds * mixed_ref[...], axis=0, keepdims=True)
        dmixed = (ds * s_ref[...]).astype(MX)
        dw = lax.dot_general(pooled_ref[...], dmixed, (((0,), (0,)), ((), ())), preferred_element_type=F32)

        @pl.when(b == 0)
        def _():
            dw_ref[0] = dw

        @pl.when(b > 0)
        def _():
            dw_ref[0] += dw
        dpooled = lax.dot_general(dmixed, w_ref[0], (((1,), (1,)), ((), ())), preferred_element_type=F32)
        row = lax.broadcasted_iota(jnp.int32, dpooled.shape, 0)
        width = lax.shift_left(jnp.int32(2), g)
        q = dpooled / jnp.minimum(row + 1, width).astype(F32)
        r2 = q + _shift_up(q, 1, row, S)
        r4 = r2 + _shift_up(r2, 2, row, S)
        r8 = r4 + _shift_up(r4, 4, row, S)
        r16 = r8 + _shift_up(r8, 8, row, S)
        win = jnp.where(g == 0, r2, jnp.where(g == 1, r4, jnp.where(g == 2, r8, r16)))
        du_ref[...] = (win - dpooled).astype(du_ref.dtype)

    return _pcall(body, name=name, grid=(POOL_G, nb),
                  in_specs=[blk, blk, blk, pl.BlockSpec((1, LANE, LANE), lambda g, b: (g, 0, 0)),
                            pl.BlockSpec((1, LANE), lambda g, b: (0, g))],
                  out_specs=[blk, pl.BlockSpec((1, LANE, LANE), lambda g, b: (g, 0, 0)),
                             pl.BlockSpec((1, 1, LANE), lambda g, b: (b, 0, g))],
                  out_shape=[_sds((T, POOL_W), MX), _sds((POOL_G, LANE, LANE), F32), _sds((nb, 1, POOL_W), F32)],
                  )(dscaled, mixed, pooled, pool_grp, pool_scale)


def _lane_masks(shape):
    lane = lax.broadcasted_iota(jnp.int32, shape, len(shape) - 1)
    m_n = lane < NOPE
    m_r = jnp.logical_and(lane >= KR_LANE, lane < KR_LANE + ROPE)
    first_half = lane < KR_LANE + ROPE // 2
    return m_n, m_r, first_half


def _rot(y, first_half):
    return jnp.where(first_half, -pltpu.roll(y, LANE - ROPE // 2, 1), pltpu.roll(y, ROPE // 2, 1))


def _rot_t(v, first_half, m_r):
    return jnp.where(m_r, jnp.where(first_half, pltpu.roll(v, LANE - ROPE // 2, 1), -pltpu.roll(v, ROPE // 2, 1)), 0.0)


def _mla_in_fwd(z, w_q, w_kv, qa_gain, kva_gain, cos, sin, q_gain, k_gain, *, name, tm=256):
    T = z.shape[0]
    slab = pl.BlockSpec((tm, LANE), lambda i: (i, 0))
    vec = pl.BlockSpec((1, LANE), lambda i: (0, 0))
    whole = lambda a: pl.BlockSpec(a.shape, lambda i: (0, 0))

    def body(z_ref, wq_ref, wkv_ref, qa_ref, kva_ref, cos_ref, sin_ref, qg_ref, kg_ref,
             qn_ref, kvn_ref, qp_ref, kvp_ref, q_ref, k_ref, v_ref):
        ql = z_ref[:, Z_QL:Z_QL + QL].astype(F32)
        kvl = z_ref[:, Z_KV:Z_KV + KVL].astype(F32)
        qn = (ql * _rsq(ql) * qa_ref[...]).astype(MX)
        kvn = (kvl * _rsq(kvl) * kva_ref[...]).astype(MX)
        qn_ref[...] = qn
        kvn_ref[...] = kvn
        qp = jnp.dot(qn, wq_ref[...], preferred_element_type=F32)
        kvp = jnp.dot(kvn, wkv_ref[...], preferred_element_type=F32)
        qp_ref[...] = qp
        kvp_ref[...] = kvp
        m_n, m_r, first_half = _lane_masks((tm, LANE))
        c = cos_ref[...]
        s = sin_ref[...]
        qg = qg_ref[...]
        kg = kg_ref[...]
        xr = z_ref[:, Z_KR:Z_KR + LANE].astype(F32)
        rr = lax.rsqrt(jnp.sum(xr * xr, axis=-1, keepdims=True) * (1.0 / ROPE) + EPS)
        yr = xr * rr * kg
        kr = jnp.where(m_r, yr * c + _rot(yr, first_half) * s, 0.0)
        for h in range(NH):
            x = qp[:, h * LANE:(h + 1) * LANE]
            x2 = x * x
            rn = lax.rsqrt(jnp.sum(jnp.where(m_n, x2, 0.0), axis=-1, keepdims=True) * (1.0 / NOPE) + EPS)
            rq = lax.rsqrt(jnp.sum(jnp.where(m_r, x2, 0.0), axis=-1, keepdims=True) * (1.0 / ROPE) + EPS)
            y = x * jnp.where(m_n, rn, jnp.where(m_r, rq, 0.0)) * qg
            q_ref[:, h * LANE:(h + 1) * LANE] = ((y * c + _rot(y, first_half) * s) * ATTN_SCALE).astype(q_ref.dtype)
            xk = kvp[:, h * LANE:(h + 1) * LANE]
            rk = lax.rsqrt(jnp.sum(jnp.where(m_n, xk * xk, 0.0), axis=-1, keepdims=True) * (1.0 / NOPE) + EPS)
            k_ref[:, h * LANE:(h + 1) * LANE] = (jnp.where(m_n, xk * rk * kg, 0.0) + kr).astype(k_ref.dtype)
        v_ref[...] = kvp[:, NH * LANE:].astype(v_ref.dtype)

    return _pcall(body, name=name, grid=(T // tm,),
                  in_specs=[_row_spec(tm, LAT_W), whole(w_q), whole(w_kv), whole(qa_gain), whole(kva_gain), slab, slab, vec, vec],
                  out_specs=[_row_spec(tm, QL), _row_spec(tm, KVL), _row_spec(tm, NH * LANE), _row_spec(tm, NH * LANE + NH * NOPE),
                             _row_spec(tm, NH * LANE), _row_spec(tm, NH * LANE), _row_spec(tm, NH * NOPE)],
                  out_shape=[_sds((T, QL), MX), _sds((T, KVL), MX), _sds((T, NH * LANE), F32), _sds((T, NH * LANE + NH * NOPE), F32),
                             _sds((T, NH * LANE), MX), _sds((T, NH * LANE), MX), _sds((T, NH * NOPE), MX)],
                  )(z, w_q, w_kv, qa_gain, kva_gain, cos, sin, q_gain, k_gain)


def _mla_in_bwd(dq, dk, dv, qp, kvp, z, w_q, w_kv, qa_gain, kva_gain, cos, sin, q_gain, k_gain, *, nb, name, tm=256):
    T = dq.shape[0]
    tps = (T // nb) // tm
    slab = pl.BlockSpec((tm, LANE), lambda i: (i, 0))
    vec = pl.BlockSpec((1, LANE), lambda i: (0, 0))
    whole = lambda a: pl.BlockSpec(a.shape, lambda i: (0, 0))

    def latent_bwd(x, dy, gain):
        r = _rsq(x)
        xhat = x * r
        dxhat = dy * gain
        return r * (dxhat - xhat * jnp.mean(dxhat * xhat, axis=-1, keepdims=True)), jnp.sum(dy * xhat, axis=0, keepdims=True)

    def body(dq_ref, dk_ref, dv_ref, qp_ref, kvp_ref, z_ref, wq_ref, wkv_ref, qa_ref, kva_ref, cos_ref, sin_ref, qg_ref, kg_ref,
             dqp_ref, dkvp_ref, dkr_ref, dql_ref, dkvl_ref, st_ref, sq_ref, sk_ref):
        i = pl.program_id(0)
        qp, kvp = qp_ref, kvp_ref
        m_n, m_r, first_half = _lane_masks((tm, LANE))
        c = cos_ref[...]
        s = sin_ref[...]
        qg = qg_ref[...]
        kg = kg_ref[...]
        dqg = jnp.zeros((1, LANE), F32)
        dkg = jnp.zeros((1, LANE), F32)
        dkr_sum = jnp.zeros((tm, LANE), F32)
        for h in range(NH):
            x = qp[:, h * LANE:(h + 1) * LANE]
            x2 = x * x
            rn = lax.rsqrt(jnp.sum(jnp.where(m_n, x2, 0.0), axis=-1, keepdims=True) * (1.0 / NOPE) + EPS)
            rq = lax.rsqrt(jnp.sum(jnp.where(m_r, x2, 0.0), axis=-1, keepdims=True) * (1.0 / ROPE) + EPS)
            rfac = jnp.where(m_n, rn, jnp.where(m_r, rq, 0.0))
            xhat = x * rfac
            do = dq_ref[:, h * LANE:(h + 1) * LANE] * ATTN_SCALE
            dy = do * c + _rot_t(do * s, first_half, m_r)
            dqg = dqg + jnp.sum(dy * xhat, axis=0, keepdims=True)
            dxhat = dy * qg
            t = dxhat * xhat
            mean_n = jnp.sum(jnp.where(m_n, t, 0.0), axis=-1, keepdims=True) * (1.0 / NOPE)
            mean_r = jnp.sum(jnp.where(m_r, t, 0.0), axis=-1, keepdims=True) * (1.0 / ROPE)
            dqp_ref[:, h * LANE:(h + 1) * LANE] = (
                rfac * (dxhat - xhat * jnp.where(m_n, mean_n, jnp.where(m_r, mean_r, 0.0)))).astype(dqp_ref.dtype)

            xk = kvp[:, h * LANE:(h + 1) * LANE]
            rk = lax.rsqrt(jnp.sum(jnp.where(m_n, xk * xk, 0.0), axis=-1, keepdims=True) * (1.0 / NOPE) + EPS)
            khat = jnp.where(m_n, xk * rk, 0.0)
            dko = dk_ref[:, h * LANE:(h + 1) * LANE]
            dkn = jnp.where(m_n, dko, 0.0)
            dkg = dkg + jnp.sum(dkn * khat, axis=0, keepdims=True)
            dkhat = dkn * kg
            mean_k = jnp.sum(dkhat * khat, axis=-1, keepdims=True) * (1.0 / NOPE)
            dkvp_ref[:, h * LANE:(h + 1) * LANE] = jnp.where(m_n, rk * (dkhat - khat * mean_k), 0.0).astype(dkvp_ref.dtype)
            dkr_sum = dkr_sum + jnp.where(m_r, dko, 0.0)
        dkvp_ref[:, NH * LANE:] = dv_ref[...].astype(dkvp_ref.dtype)

        xr = z_ref[:, Z_KR:Z_KR + LANE].astype(F32)
        rr = lax.rsqrt(jnp.sum(xr * xr, axis=-1, keepdims=True) * (1.0 / ROPE) + EPS)
        rhat = xr * rr
        dyr = dkr_sum * c + _rot_t(dkr_sum * s, first_half, m_r)
        dkg = dkg + jnp.sum(dyr * rhat, axis=0, keepdims=True)
        drhat = dyr * kg
        mean_kr = jnp.sum(drhat * rhat, axis=-1, keepdims=True) * (1.0 / ROPE)
        dkr_ref[...] = jnp.where(m_r, rr * (drhat - rhat * mean_kr), 0.0).astype(dkr_ref.dtype)

        nt = (((1,), (1,)), ((), ()))
        dqn = lax.dot_general(dqp_ref[...], wq_ref[...], nt, preferred_element_type=F32)
        dkvn = lax.dot_general(dkvp_ref[...], wkv_ref[...], nt, preferred_element_type=F32)
        dql, dqa = latent_bwd(z_ref[:, Z_QL:Z_QL + QL].astype(F32), dqn, qa_ref[...])
        dkvl, dkva = latent_bwd(z_ref[:, Z_KV:Z_KV + KVL].astype(F32), dkvn, kva_ref[...])
        dql_ref[...] = dql.astype(dql_ref.dtype)
        dkvl_ref[...] = dkvl.astype(dkvl_ref.dtype)

        @pl.when(i % tps == 0)
        def _():
            st_ref[...] = jnp.zeros_like(st_ref)
            sq_ref[...] = jnp.zeros_like(sq_ref)
            sk_ref[...] = jnp.zeros_like(sk_ref)

        st_ref[0, 0:1, :] += dqg
        st_ref[0, 1:2, :] += dkg
        sq_ref[0, 0:1, :] += dqa
        sk_ref[0, 0:1, :] += dkva

    stat = lambda w: pl.BlockSpec((1, SUBLANE, w), lambda i: (i // tps, 0, 0))
    return _pcall(body, name=name, grid=(T // tm,),
                  in_specs=[_row_spec(tm, NH * LANE), _row_spec(tm, NH * LANE), _row_spec(tm, NH * NOPE),
                            _row_spec(tm, NH * LANE), _row_spec(tm, NH * LANE + NH * NOPE), _row_spec(tm, LAT_W),
                            whole(w_q), whole(w_kv), whole(qa_gain), whole(kva_gain), slab, slab, vec, vec],
                  out_specs=[_row_spec(tm, NH * LANE), _row_spec(tm, NH * LANE + NH * NOPE), slab, _row_spec(tm, QL),
                             _row_spec(tm, KVL), stat(LANE), stat(QL), stat(KVL)],
                  out_shape=[_sds((T, NH * LANE), MX), _sds((T, NH * LANE + NH * NOPE), MX), _sds((T, LANE), MX),
                             _sds((T, QL), MX), _sds((T, KVL), MX), _sds((nb, SUBLANE, LANE), F32),
                             _sds((nb, SUBLANE, QL), F32), _sds((nb, SUBLANE, KVL), F32)],
                  )(dq, dk, dv, qp, kvp, z, w_q, w_kv, qa_gain, kva_gain, cos, sin, q_gain, k_gain)


def _lower_triangle(t):
    return lax.broadcasted_iota(jnp.int32, (t, t), 1) <= lax.broadcasted_iota(jnp.int32, (t, t), 0)


def _attn_fwd(q, k, v, *, nb, name, tq=512, comm=None):
    T = q.shape[0]
    S = T // nb
    tq = _pick(S, tq)
    nq = S // tq
    tk = tq
    npair = NH // 2

    def body(q_ref, k_ref, v_ref, o_ref, lse_ref):
        qi = pl.program_id(2)
        lane = lax.broadcasted_iota(jnp.int32, (tq, LANE), 1)
        qs = [q_ref[:, hh * LANE:(hh + 1) * LANE] for hh in range(2)]

        def block(j, carry, diagonal):
            k0 = pl.multiple_of(j * tk, tk)
            vb = v_ref[pl.ds(k0, tk), :]
            new = []
            for hh in range(2):
                m, l, acc = carry[hh]
                kb = k_ref[pl.ds(k0, tk), hh * LANE:(hh + 1) * LANE]
                s = lax.dot_general(qs[hh], kb, (((1,), (1,)), ((), ())), preferred_element_type=F32)
                if diagonal:
                    s = jnp.where(_lower_triangle(tq), s, NEG)
                m_new = jnp.maximum(m, jnp.max(s, axis=-1, keepdims=True))
                p = jnp.exp(s - m_new)
                alpha = jnp.exp(m - m_new)
                l = alpha * l + jnp.sum(p, axis=-1, keepdims=True)
                acc = alpha * acc + jnp.dot(p.astype(MX), vb, preferred_element_type=F32)
                new.append((m_new, l, acc))
            return tuple(new)

        init = tuple((jnp.full((tq, 1), NEG, F32), jnp.zeros((tq, 1), F32), jnp.zeros((tq, LANE), F32)) for _ in range(2))
        carry = lax.fori_loop(0, qi, lambda j, c: block(j, c, False), init)
        (m0, l0, acc0), (m1, l1, acc1) = block(qi, carry, True)
        o_ref[...] = jnp.where(lane < NOPE, acc0 / l0, acc1 / l1).astype(o_ref.dtype)
        lse_ref[...] = jnp.where(lane < NOPE, m0 + jnp.log(l0), m1 + jnp.log(l1))

    return _pcall(body, name=name, grid=(nb, npair, nq),
                  in_specs=[pl.BlockSpec((tq, 2 * LANE), lambda b, p, i: (b * nq + i, p)),
                            pl.BlockSpec((S, 2 * LANE), lambda b, p, i: (b, p)),
                            pl.BlockSpec((S, LANE), lambda b, p, i: (b, p))],
                  out_specs=[pl.BlockSpec((tq, LANE), lambda b, p, i: (b * nq + i, p)),
                             pl.BlockSpec((tq, LANE), lambda b, p, i: (b * nq + i, p))],
                  out_shape=[_sds((T, NH * NOPE), MX), _sds((T, NH * NOPE), F32)], comm=comm)(q, k, v)


def _attn_bwd(q, k, v, o, lse, do, *, nb, name, tq=512, comm=None):
    T = q.shape[0]
    S = T // nb
    tq = _pick(S, tq)
    nq = S // tq
    tk = tq
    npair = NH // 2

    def body(q_ref, k_ref, v_ref, o_ref, lse_ref, do_ref, dq_ref, dk_ref, dv_ref, delta_ref):
        lane = lax.broadcasted_iota(jnp.int32, (tq, LANE), 1)
        first = lane < NOPE
        dq_ref[...] = jnp.zeros_like(dq_ref)

        def delta_step(qi, _):
            q0 = pl.multiple_of(qi * tq, tq)
            prod = do_ref[pl.ds(q0, tq), :] * o_ref[pl.ds(q0, tq), :].astype(F32)
            d0 = jnp.sum(jnp.where(first, prod, 0.0), axis=-1, keepdims=True)
            d1 = jnp.sum(jnp.where(first, 0.0, prod), axis=-1, keepdims=True)
            delta_ref[pl.ds(q0, tq), :] = jnp.where(first, d0, d1)
            return 0

        lax.fori_loop(0, nq, delta_step, 0)

        def kv_step(kj, _):
            k0 = pl.multiple_of(kj * tk, tk)
            kbs = [k_ref[pl.ds(k0, tk), hh * LANE:(hh + 1) * LANE] for hh in range(2)]
            vb = v_ref[pl.ds(k0, tk), :]

            def q_block(qi, carry, diagonal):
                dk0, dk1, dv = carry
                dks = [dk0, dk1]
                q0 = pl.multiple_of(qi * tq, tq)
                dov = do_ref[pl.ds(q0, tq), :]
                lse_v = lse_ref[pl.ds(q0, tq), :]
                delta_v = delta_ref[pl.ds(q0, tq), :]
                for hh in range(2):
                    qh = q_ref[pl.ds(q0, tq), hh * LANE:(hh + 1) * LANE]
                    dob = jnp.where(first if hh == 0 else jnp.logical_not(first), dov, 0.0).astype(MX)
                    s = lax.dot_general(qh, kbs[hh], (((1,), (1,)), ((), ())), preferred_element_type=F32)
                    p = jnp.exp(s - lse_v[:, hh * NOPE:hh * NOPE + 1])
                    if diagonal:
                        p = jnp.where(_lower_triangle(tq), p, 0.0)
                    dp = lax.dot_general(dob, vb, (((1,), (1,)), ((), ())), preferred_element_type=F32)
                    ds = (p * (dp - delta_v[:, hh * NOPE:hh * NOPE + 1])).astype(MX)
                    dks[hh] = dks[hh] + lax.dot_general(ds, qh, (((0,), (0,)), ((), ())), preferred_element_type=F32)
                    dv = dv + lax.dot_general(p.astype(MX), dob, (((0,), (0,)), ((), ())), preferred_element_type=F32)
                    dq_ref[pl.ds(q0, tq), hh * LANE:(hh + 1) * LANE] += jnp.dot(ds, kbs[hh], preferred_element_type=F32)
                return dks[0], dks[1], dv

            zero = jnp.zeros((tk, LANE), F32)
            carry = q_block(kj, (zero, zero, zero), True)
            dk0, dk1, dv = lax.fori_loop(kj + 1, nq, lambda qi, c: q_block(qi, c, False), carry)
            dk_ref[pl.ds(k0, tk), 0:LANE] = dk0
            dk_ref[pl.ds(k0, tk), LANE:2 * LANE] = dk1
            dv_ref[pl.ds(k0, tk), :] = dv
            return 0

        lax.fori_loop(0, nq, kv_step, 0)

    pair256 = pl.BlockSpec((S, 2 * LANE), lambda b, p: (b, p))
    pair128 = pl.BlockSpec((S, LANE), lambda b, p: (b, p))
    return _pcall(body, name=name, grid=(nb, npair),
                  in_specs=[pair256, pair256, pair128, pair128, pair128, pair128],
                  out_specs=[pair256, pair256, pair128],
                  out_shape=[_sds((T, NH * LANE), F32), _sds((T, NH * LANE), F32), _sds((T, NH * NOPE), F32)],
                  scratch=[pltpu.VMEM((S, LANE), F32)], comm=comm)(q, k, v, o, lse, do)


def _run(plan, fn, *args, name, **kw):
    rider = plan.rider(name)
    if rider is None:
        return fn(*args, name=name, **kw)
    outs, landed = fn(*args, name=name, comm=rider, **kw)
    plan.landed(name, landed)
    return outs


def _layer_fwd_bwd(x, tgt, mod3, cos, sin, plan, P):
    nb = mod3.shape[0]
    W = plan.W
    h1, gu1, a1 = _run(plan, _ffn_up_first, x, mod3, P["norm_ffn1"], W["ffn1_in"], sub=0, name="ffn1_up_a")
    gu1, a1 = _run(plan, _ffn_up_second, h1, W["ffn1_in"], gu1, a1, name="ffn1_up_b")
    f1, x1, h2 = _run(plan, _mm_resid_norm, a1, W["ffn1_out"], x, mod3, P["norm_mix"], sub=1, coef=0.5, name="ffn1_out")
    z = _run(plan, _mm, h2, W["w_in"], mode="nn", out_dtype=MX, name="mix_in", tn=1664)
    pooled, mixed, scaled = _pool_fwd(z, W["pool_grp"], P["pool_scale"], nb=nb, name="pool_fwd")
    br_pool = _mm(scaled, W["pool_proj"], mode="nn", out_dtype=MX, name="pool_proj")
    qn, kvn, qp, kvp, q, k, v = _mla_in_fwd(z, W["q_up"], W["kv_up"], P["q_a_norm"], P["kv_a_norm"], cos, sin,
                                            P["q_gain"], P["k_gain"], name="mla_in_fwd")
    attn, lse = _run(plan, _attn_fwd, q, k, v, nb=nb, name="attn_fwd")
    br_mla, merged, mo, x2, h3 = _mix_out_fwd(z, br_pool, attn, W["mla_proj"], W["w_out"], x1, mod3, P["norm_ffn2"], sub=2,
                                              name="mix_out", tm=256)
    gu2, a2 = _ffn_up(h3, W["ffn2_in"], name="ffn2_up")
    dy, df2, st_fin, loss = _mm_loss(a2, W["ffn2_out"], x2, tgt, mod3, name="ffn2_out_loss")

    plan.grad("ffn2_out", _dw(a2, df2, name="d_ffn2_out"))
    dgu2 = _ffn_dact(df2, gu2, W["ffn2_out"], name="ffn2_dact")
    plan.grad("ffn2_in", _ffn_dw_in(h3, dgu2, name="d_ffn2_in"))
    dx2, dmo, st3 = _run(plan, _ffn_dh_norm, dgu2, W["ffn2_in"], x2, dy, mo, mod3, P["norm_ffn2"], sub=2, coef=1.0,
                         name="d_ffn2_h")
    plan.grad("w_out", _dw(merged, dmo, name="d_mix_out"))
    dbr_pool, dbr_mla, dgates, dattn = _mix_out_bwd(dmo, W["w_out"], z, br_pool, br_mla, W["mla_proj"], name="mix_out_bwd", tm=256)
    plan.grad("pool_proj", _dw(scaled, dbr_pool, name="d_pool_proj"))
    dscaled = _mm(dbr_pool, W["pool_proj"], mode="nt", out_dtype=F32, name="d_pool_scaled")
    du_pool, d_pool_grp, d_pool_scale = _pool_bwd(dscaled, mixed, pooled, W["pool_grp"], P["pool_scale"], nb=nb, name="pool_bwd")
    plan.grad("mla_proj", _dw(attn, dbr_mla, name="d_mla_proj"))
    dq, dk, dv = _run(plan, _attn_bwd, q, k, v, attn, lse, dattn, nb=nb, name="attn_bwd")
    dqp, dkvp, dkr, dql, dkvl, st_prep, st_q, st_kv = _mla_in_bwd(
        dq, dk, dv, qp, kvp, z, W["q_up"], W["kv_up"], P["q_a_norm"], P["kv_a_norm"], cos, sin, P["q_gain"], P["k_gain"],
        nb=nb, name="mla_in_bwd")
    plan.grad("q_up", _dw(qn, dqp, name="d_q_up"))
    plan.grad("kv_up", _dw(kvn, dkvp, name="d_kv_up"))
    dz = jnp.concatenate([du_pool, dql, dkvl, dkr, dgates], axis=1)
    plan.grad("w_in", _run(plan, _dw, h2, dz, name="d_mix_in", tm=256, tn=1664))
    dx1, df1, st2 = _run(plan, _mm_nt_norm, dz, W["w_in"], x1, dx2, f1, mod3, P["norm_mix"], sub=1, coef=0.5, name="d_mix_h")
    plan.grad("ffn1_out", _run(plan, _dw, a1, df1, name="d_ffn1_out"))
    dgu1 = _run(plan, _ffn_dact, df1, gu1, W["ffn1_out"], name="ffn1_dact")
    half = x.shape[1] // 2
    plan.grad("ffn1_in@0", _run(plan, _ffn_dw_in, h1, dgu1, rows=(0, half), name="d_ffn1_in_a"))
    plan.grad("ffn1_in@1", _run(plan, _ffn_dw_in, h1, dgu1, rows=(half, half), name="d_ffn1_in_b"))
    dh1 = _run(plan, _ffn_dh, dgu1, W["ffn1_in"], name="d_ffn1_h")
    grad_x, st1 = _run(plan, _bwd_block, x, dh1, dx1, None, mod3, P["norm_ffn1"], sub=0, coef=0.0, name="bwd_norm1")

    return loss, grad_x, (st1, st2, st3, st_fin, st_q, st_kv, st_prep, d_pool_scale), d_pool_grp


def _w_in_to_kernel(w):
    k = w.shape[0]
    zeros = lambda n: jnp.zeros((k, n), w.dtype)
    return jnp.concatenate([w[:, 0:1152], zeros(KR_LANE), w[:, 1152:1184], zeros(LANE - KR_LANE - ROPE), w[:, 1184:]], axis=1)


def _w_in_from_kernel(g):
    return jnp.concatenate([g[:, 0:1152], g[:, Z_KR + KR_LANE:Z_KR + KR_LANE + ROPE], g[:, Z_GP:]], axis=1)


def _q_up_to_kernel(w):
    k = w.shape[0]
    return jnp.pad(w.reshape(k, NH, NOPE + ROPE), ((0, 0), (0, 0), (0, LANE - NOPE - ROPE))).reshape(k, NH * LANE)


def _q_up_from_kernel(g):
    k = g.shape[0]
    return g.reshape(k, NH, LANE)[:, :, :NOPE + ROPE].reshape(k, NH * (NOPE + ROPE))


def _kv_up_to_kernel(w):
    k = w.shape[0]
    w3 = w.reshape(k, NH, 2 * NOPE)
    kpart = jnp.pad(w3[:, :, :NOPE], ((0, 0), (0, 0), (0, LANE - NOPE))).reshape(k, NH * LANE)
    return jnp.concatenate([kpart, w3[:, :, NOPE:].reshape(k, NH * NOPE)], axis=1)


def _kv_up_from_kernel(g):
    k = g.shape[0]
    kpart = g[:, :NH * LANE].reshape(k, NH, LANE)[:, :, :NOPE]
    vpart = g[:, NH * LANE:].reshape(k, NH, NOPE)
    return jnp.concatenate([kpart, vpart], axis=2).reshape(k, NH * 2 * NOPE)


def _gain_slab(nope, rope):
    return jnp.concatenate([nope, rope, jnp.zeros((1, LANE - NOPE - ROPE), nope.dtype)], axis=1)


def _rope_tables(positions):
    inv_freq = 10000.0 ** (-jnp.arange(0, ROPE, 2, dtype=F32) / ROPE)
    ang = positions.astype(F32)[:, None] * inv_freq
    ang = jnp.concatenate([ang, ang], axis=-1)
    t = positions.shape[0]
    cos = jnp.concatenate([jnp.ones((t, KR_LANE), F32), jnp.cos(ang), jnp.ones((t, LANE - KR_LANE - ROPE), F32)], axis=1)
    sin = jnp.concatenate([jnp.zeros((t, KR_LANE), F32), jnp.sin(ang), jnp.zeros((t, LANE - KR_LANE - ROPE), F32)], axis=1)
    return cos, sin


def _coords():
    return lax.axis_index("x"), lax.axis_index("y"), lax.axis_index("c")


HBM_SPEC = pl.BlockSpec(memory_space=pl.ANY)
VMEM_SPEC = pl.BlockSpec(memory_space=pltpu.VMEM)


CHIP_RELS = ((1, 0), (0, 1), (1, 1))


def _gather_comm(shards):
    n = len(shards)

    def ici(ins, outs, sems, a, j, x, y, cc):
        half = ins[a].shape[0] // 2
        mine = pl.ds(cc * half, half)
        dx, dy = CHIP_RELS[j]
        return pltpu.make_async_remote_copy(src_ref=ins[a].at[mine], dst_ref=outs[a].at[2 * x + y, mine],
                                            send_sem=sems[0].at[a, j], recv_sem=sems[1].at[a, j],
                                            device_id=(x ^ dx, y ^ dy, cc), device_id_type=MESH)

    def d2d(ins, outs, sems, a, j, x, y, cc, half_of):
        half = ins[a].shape[0] // 2
        dx, dy = CHIP_RELS[j]
        landed = outs[a].at[2 * (x ^ dx) + (y ^ dy), pl.ds(half_of * half, half)]
        return pltpu.make_async_remote_copy(src_ref=landed, dst_ref=landed, send_sem=sems[2].at[a, j], recv_sem=sems[3].at[a, j],
                                            device_id=(x, y, 1 - cc), device_id_type=MESH)

    def own(ins, outs, sems, a, x, y):
        return pltpu.make_async_copy(ins[a], outs[a].at[2 * x + y], sems[4].at[a])

    def start(ins, outs, sems):
        x, y, cc = _coords()
        for a in range(n):
            own(ins, outs, sems, a, x, y).start()
            for j in range(3):
                ici(ins, outs, sems, a, j, x, y, cc).start()

    def finish(ins, outs, sems):
        x, y, cc = _coords()
        for a in range(n):
            for j in range(3):
                ici(ins, outs, sems, a, j, x, y, cc).wait_recv()
                d2d(ins, outs, sems, a, j, x, y, cc, cc).start()
        for a in range(n):
            for j in range(3):
                d2d(ins, outs, sems, a, j, x, y, cc, 1 - cc).wait_recv()
        for a in range(n):
            for j in range(3):
                ici(ins, outs, sems, a, j, x, y, cc).wait_send()
                d2d(ins, outs, sems, a, j, x, y, cc, cc).wait_send()
            own(ins, outs, sems, a, x, y).wait()

    dma = pltpu.SemaphoreType.DMA
    return _Comm(shards, [_sds((N_CHIP,) + s.shape, s.dtype) for s in shards],
                 [dma((n, 3)), dma((n, 3)), dma((n, 3)), dma((n, 3)), dma((n,))], start, finish)


def _swap_comm(parts):
    n = len(parts)

    def copy(ins, outs, sems, a):
        x, y, cc = _coords()
        half = ins[a].shape[1] // 2
        return pltpu.make_async_remote_copy(src_ref=ins[a].at[:, pl.ds((1 - cc) * half, half)], dst_ref=outs[a],
                                            send_sem=sems[0].at[a], recv_sem=sems[1].at[a], device_id=(x, y, 1 - cc),
                                            device_id_type=MESH)

    def start(ins, outs, sems):
        for a in range(n):
            copy(ins, outs, sems, a).start()

    def finish(ins, outs, sems):
        for a in range(n):
            copy(ins, outs, sems, a).wait()

    dma = pltpu.SemaphoreType.DMA
    return _Comm(parts, [_sds((p.shape[0], p.shape[1] // 2, p.shape[2]), p.dtype) for p in parts], [dma((n,)), dma((n,))],
                 start, finish)


def _add_half(full, other, cidx, *, name):
    nch, r, c = full.shape
    half = r // 2
    tr = _pick_rows(half)
    nbk = half // tr
    grid_spec = pltpu.PrefetchScalarGridSpec(
        num_scalar_prefetch=1, grid=(nch, nbk),
        in_specs=[pl.BlockSpec((1, tr, c), lambda j, i, cref: (j, cref[0] * nbk + i, 0)),
                  pl.BlockSpec((1, tr, c), lambda j, i, cref: (j, i, 0))],
        out_specs=pl.BlockSpec((1, tr, c), lambda j, i, cref: (j, i, 0)))

    def body(cref, a_ref, b_ref, o_ref):
        o_ref[...] = (a_ref[...] + b_ref[...]).astype(o_ref.dtype)

    return _pcall(body, name=name, out_shape=_sds((nch, half, c), MX), grid_spec=grid_spec)(cidx, full, other)


def _pick_rows(rows, target=512):
    best = None
    for t in range(16, min(rows, target) + 1, 16):
        if rows % t == 0:
            best = t
    return rows if best is None else best


def _exchange_comm(parts):
    n = len(parts)

    def send(ins, outs, sems, a, j, x, y, cc):
        dx, dy = CHIP_RELS[j]
        return pltpu.make_async_remote_copy(src_ref=ins[a].at[2 * (x ^ dx) + (y ^ dy)], dst_ref=outs[a].at[2 * x + y],
                                            send_sem=sems[0].at[a, j], recv_sem=sems[1].at[a, j],
                                            device_id=(x ^ dx, y ^ dy, cc), device_id_type=MESH)

    def landing(ins, outs, sems, a, j, x, y, cc):
        dx, dy = CHIP_RELS[j]
        peer_chip = 2 * (x ^ dx) + (y ^ dy)
        return pltpu.make_async_remote_copy(src_ref=ins[a].at[peer_chip], dst_ref=outs[a].at[peer_chip], send_sem=sems[0].at[a, j],
                                            recv_sem=sems[1].at[a, j], device_id=(x, y, cc), device_id_type=MESH)

    def own(ins, outs, sems, a, x, y):
        return pltpu.make_async_copy(ins[a].at[2 * x + y], outs[a].at[2 * x + y], sems[2].at[a])

    def start(ins, outs, sems):
        x, y, cc = _coords()
        for a in range(n):
            own(ins, outs, sems, a, x, y).start()
            for j in range(3):
                send(ins, outs, sems, a, j, x, y, cc).start()

    def finish(ins, outs, sems):
        x, y, cc = _coords()
        for a in range(n):
            for j in range(3):
                landing(ins, outs, sems, a, j, x, y, cc).wait_recv()
        for a in range(n):
            for j in range(3):
                send(ins, outs, sems, a, j, x, y, cc).wait_send()
            own(ins, outs, sems, a, x, y).wait()

    dma = pltpu.SemaphoreType.DMA
    return _Comm(parts, [_sds(p.shape, p.dtype) for p in parts], [dma((n, 3)), dma((n, 3)), dma((n,))], start, finish)


def _sum_chips(q, cidx, *, name):
    nch, h, c = q.shape
    tr = _pick_rows(h)
    nbk = h // tr
    grid_spec = pltpu.PrefetchScalarGridSpec(
        num_scalar_prefetch=1, grid=(nbk,),
        in_specs=[pl.BlockSpec((nch, tr, c), lambda i, cref: (0, i, 0))],
        out_specs=pl.BlockSpec((tr, c), lambda i, cref: (cref[0] * nbk + i, 0)))

    def body(cref, q_ref, o_ref):
        acc = q_ref[0].astype(F32) + q_ref[1].astype(F32)
        acc = acc + q_ref[2].astype(F32)
        o_ref[...] = acc + q_ref[3].astype(F32)

    return _pcall(body, name=name, out_shape=_sds((2 * h, c), F32), grid_spec=grid_spec)(cidx, q)


def _join_comm(fulls):
    n = len(fulls)

    def half_copy(outs, sems, a, which):
        x, y, cc = _coords()
        h = outs[a].shape[0] // 2
        rows = outs[a].at[pl.ds((cc if which == 0 else 1 - cc) * h, h)]
        return pltpu.make_async_remote_copy(src_ref=rows, dst_ref=rows, send_sem=sems[0].at[a], recv_sem=sems[1].at[a],
                                            device_id=(x, y, 1 - cc), device_id_type=MESH)

    def start(ins, outs, sems):
        for a in range(n):
            half_copy(outs, sems, a, 0).start()

    def finish(ins, outs, sems):
        for a in range(n):
            half_copy(outs, sems, a, 1).wait_recv()
        for a in range(n):
            half_copy(outs, sems, a, 0).wait_send()

    dma = pltpu.SemaphoreType.DMA
    return _Comm(fulls, [_sds(p.shape, p.dtype) for p in fulls], [dma((n,)), dma((n,))], start, finish,
                 aliases={a: a for a in range(n)})


def _ada_prologue(c, w, b, *, name, comm=None):
    nb, dm = c.shape
    n = w.shape[1]

    def body(c_ref, w_ref, b_ref, call_ref, land_ref, cpad, mod_s, g_send, g_recv, m_send, m_recv):
        x, y, cc = _coords()
        me = 4 * x + 2 * y + cc
        chip = 2 * x + y
        cpad[...] = jnp.zeros_like(cpad)
        cpad[0:nb, :] = c_ref[...]
        copies = []
        for kk in range(1, N_DEV):
            peer = (x ^ (kk >> 2), y ^ ((kk >> 1) & 1), cc ^ (kk & 1))
            cp = pltpu.make_async_remote_copy(src_ref=cpad, dst_ref=call_ref.at[me], send_sem=g_send.at[kk - 1],
                                              recv_sem=g_recv.at[kk - 1], device_id=peer, device_id_type=MESH)
            cp.start()
            copies.append(cp)
        call_ref[me] = cpad[...]
        for kk in range(1, N_DEV):
            pltpu.make_async_remote_copy(src_ref=cpad, dst_ref=call_ref.at[me ^ kk], send_sem=g_send.at[kk - 1],
                                         recv_sem=g_recv.at[kk - 1], device_id=(x, y, cc), device_id_type=MESH).wait_recv()
        cv = call_ref[...].reshape(N_DEV * SUBLANE, dm)
        act = (cv * _sigmoid(cv)).astype(MX)
        mod = jnp.dot(act, w_ref[...].astype(MX), preferred_element_type=F32) + b_ref[...]
        mod_s[...] = mod.reshape(N_DEV, SUBLANE, n)
        for j, (dx, dy) in enumerate(CHIP_RELS):
            cp = pltpu.make_async_remote_copy(src_ref=mod_s.at[4 * (x ^ dx) + 2 * (y ^ dy) + cc], dst_ref=land_ref.at[chip],
                                              send_sem=m_send.at[j], recv_sem=m_recv.at[j],
                                              device_id=(x ^ dx, y ^ dy, cc), device_id_type=MESH)
            cp.start()
            copies.append(cp)
        land_ref[chip] = mod_s[me]
        for j, (dx, dy) in enumerate(CHIP_RELS):
            pltpu.make_async_remote_copy(src_ref=mod_s.at[me], dst_ref=land_ref.at[2 * (x ^ dx) + (y ^ dy)], send_sem=m_send.at[j],
                                         recv_sem=m_recv.at[j], device_id=(x, y, cc), device_id_type=MESH).wait_recv()
        for cp in copies:
            cp.wait_send()

    dma = pltpu.SemaphoreType.DMA
    return _pcall(body, name=name, in_specs=[VMEM_SPEC] * 3, out_specs=[VMEM_SPEC] * 2,
                  out_shape=[_sds((N_DEV, SUBLANE, dm), F32), _sds((N_CHIP, SUBLANE, n), F32)],
                  scratch=[pltpu.VMEM((SUBLANE, dm), F32), pltpu.VMEM((N_DEV, SUBLANE, n), F32),
                           dma((N_DEV - 1,)), dma((N_DEV - 1,)), dma((3,)), dma((3,))], comm=comm)(c, w, b)


def _ada_bwd(c_all, dmod_cols, *, name):
    m, kdim = c_all.shape
    n = dmod_cols.shape[1]
    tn = _pick(n, 1152)

    def body(c_ref, d_ref, o_ref):
        cv = c_ref[...]
        act = (cv * _sigmoid(cv)).astype(MX)
        o_ref[...] = lax.dot_general(act, d_ref[...].astype(MX), (((0,), (0,)), ((), ())), preferred_element_type=F32)

    return _pcall(body, name=name, out_shape=_sds((kdim, n), F32), grid=(n // tn,),
                  in_specs=[pl.BlockSpec((m, kdim), lambda j: (0, 0)), pl.BlockSpec((m, tn), lambda j: (0, j))],
                  out_specs=pl.BlockSpec((kdim, tn), lambda j: (0, j)))(c_all, dmod_cols)


SLAB_W = 1024
RED_ROWS = 8


LOSS_LANE = SLAB_W - LANE


def _pack_stats(st1, st2, st3, st_fin, st_q, st_kv, st_prep, d_pool_scale, loss8, *, name):
    nb = st1.shape[0]
    dm_rows = -(-9 * nb // SUBLANE) * SUBLANE

    def body(s1, s2, s3, sf, sq, skv, sp, sps, loss_ref, o_ref):
        o_ref[...] = jnp.zeros_like(o_ref)
        for s in range(nb):
            rows = [s1[s, 0:1, :], s1[s, 1:2, :], s2[s, 3:4, :], s2[s, 0:1, :], s2[s, 1:2, :], s3[s, 3:4, :], s3[s, 0:1, :],
                    s3[s, 1:2, :], sf[s, 0:1, :]]
            for k, row in enumerate(rows):
                o_ref[9 * s + k:9 * s + k + 1, :] = row

        def over_seq(ref, r):
            acc = ref[0, r:r + 1, :]
            for s in range(1, nb):
                acc = acc + ref[s, r:r + 1, :]
            return acc

        o_ref[dm_rows + 0:dm_rows + 1, :] = over_seq(s1, 2)
        o_ref[dm_rows + 1:dm_rows + 2, :] = over_seq(s2, 2)
        o_ref[dm_rows + 2:dm_rows + 3, :] = over_seq(s3, 2)
        o_ref[dm_rows + 3:dm_rows + 4, 0:POOL_W] = over_seq(sps, 0)
        o_ref[dm_rows + 4:dm_rows + 5, 0:QL] = over_seq(sq, 0)
        o_ref[dm_rows + 5:dm_rows + 6, 0:KVL] = over_seq(skv, 0)
        o_ref[dm_rows + 6:dm_rows + 7, 0:LANE] = over_seq(sp, 0)
        o_ref[dm_rows + 7:dm_rows + 8, 0:LANE] = over_seq(sp, 1)
        o_ref[dm_rows + 7:dm_rows + 8, LOSS_LANE:] = loss_ref[0:1, :]

    return _pcall(body, name=name, out_shape=_sds((dm_rows + RED_ROWS, SLAB_W), F32), in_specs=[VMEM_SPEC] * 9,
                  out_specs=VMEM_SPEC)(st1, st2, st3, st_fin, st_q, st_kv, st_prep, d_pool_scale, loss8)


def _small_allreduce(slab, pool, *, name, comm=None):
    rows, w = slab.shape
    dm = rows - RED_ROWS
    prow, pw = pool.shape
    crow = RED_ROWS + 2 * dm

    def body(slab_ref, pool_ref, red_ref, dmod_ref, ptot_ref, sib_slab, sib_pool, chip_slab, chip_pool, land_slab, land_pool,
             a_send, a_recv, b_send, b_recv):
        x, y, cc = _coords()
        chip = 2 * x + y
        sib = (x, y, 1 - cc)
        to_sib = [pltpu.make_async_remote_copy(src_ref=slab_ref, dst_ref=sib_slab, send_sem=a_send.at[0], recv_sem=a_recv.at[0],
                                               device_id=sib, device_id_type=MESH),
                  pltpu.make_async_remote_copy(src_ref=pool_ref, dst_ref=sib_pool, send_sem=a_send.at[1], recv_sem=a_recv.at[1],
                                               device_id=sib, device_id_type=MESH)]
        for cp in to_sib:
            cp.start()
        for cp in to_sib:
            cp.wait()
        mine_dm, theirs_dm = slab_ref[0:dm, :], sib_slab[0:dm, :]
        chip_slab[0:RED_ROWS, :] = slab_ref[dm:, :] + sib_slab[dm:, :]
        chip_slab[RED_ROWS:RED_ROWS + dm, :] = jnp.where(cc == 0, mine_dm, theirs_dm)
        chip_slab[RED_ROWS + dm:, :] = jnp.where(cc == 0, theirs_dm, mine_dm)
        chip_pool[...] = pool_ref[...] + sib_pool[...]

        sends = []
        for j, (dx, dy) in enumerate(CHIP_RELS):
            peer = (x ^ dx, y ^ dy, cc)
            sends.append(pltpu.make_async_remote_copy(src_ref=chip_slab, dst_ref=land_slab.at[chip], send_sem=b_send.at[j, 0],
                                                      recv_sem=b_recv.at[j, 0], device_id=peer, device_id_type=MESH))
            sends.append(pltpu.make_async_remote_copy(src_ref=chip_pool, dst_ref=land_pool.at[chip], send_sem=b_send.at[j, 1],
                                                      recv_sem=b_recv.at[j, 1], device_id=peer, device_id_type=MESH))
        for cp in sends:
            cp.start()
        land_slab[chip] = chip_slab[...]
        land_pool[chip] = chip_pool[...]
        for j, (dx, dy) in enumerate(CHIP_RELS):
            peer_chip = 2 * (x ^ dx) + (y ^ dy)
            pltpu.make_async_remote_copy(src_ref=chip_slab, dst_ref=land_slab.at[peer_chip], send_sem=b_send.at[j, 0],
                                         recv_sem=b_recv.at[j, 0], device_id=(x, y, cc), device_id_type=MESH).wait_recv()
            pltpu.make_async_remote_copy(src_ref=chip_pool, dst_ref=land_pool.at[peer_chip], send_sem=b_send.at[j, 1],
                                         recv_sem=b_recv.at[j, 1], device_id=(x, y, cc), device_id_type=MESH).wait_recv()
        red = land_slab[0, 0:RED_ROWS, :]
        ptot = land_pool[0]
        for ch in range(1, N_CHIP):
            red = red + land_slab[ch, 0:RED_ROWS, :]
            ptot = ptot + land_pool[ch]
        red_ref[...] = red
        ptot_ref[...] = ptot
        for ch in range(N_CHIP):
            dmod_ref[2 * dm * ch:2 * dm * (ch + 1), :] = land_slab[ch, RED_ROWS:, :]
        for cp in sends:
            cp.wait_send()

    dma = pltpu.SemaphoreType.DMA
    return _pcall(body, name=name, in_specs=[VMEM_SPEC, VMEM_SPEC], out_specs=[VMEM_SPEC] * 3,
                  out_shape=[_sds((RED_ROWS, w), F32), _sds((N_DEV * dm, w), F32), _sds((prow, pw), F32)],
                  scratch=[pltpu.VMEM((rows, w), F32), pltpu.VMEM((prow, pw), F32), pltpu.VMEM((crow, w), F32),
                           pltpu.VMEM((prow, pw), F32), pltpu.VMEM((N_CHIP, crow, w), F32), pltpu.VMEM((N_CHIP, prow, pw), F32),
                           dma((2,)), dma((2,)), dma((3, 2)), dma((3, 2))], comm=comm)(slab, pool)


def _adamw_math(w, g, m, v):
    mn = ADAM_B1 * m + (1.0 - ADAM_B1) * g
    vn = ADAM_B2 * v + (1.0 - ADAM_B2) * (g * g)
    bc1 = 1.0 / (1.0 - ADAM_B1 ** ADAM_STEP)
    bc2 = 1.0 / (1.0 - ADAM_B2 ** ADAM_STEP)
    return -ADAM_LR * ((mn * bc1) / (jnp.sqrt(vn * bc2) + ADAM_EPS) + ADAM_WD * w), mn, vn


def _small_update(red, dmod_all, pool_total, nb, params, *, name):
    names = list(SMALL)
    dm = dmod_all.shape[0] // N_DEV

    def grad_of(nm, red_ref, dmod_ref, ptot_ref):
        if nm == "b_ada":
            acc = None
            for d in range(N_DEV):
                for s in range(nb):
                    blk = dmod_ref[d * dm + 9 * s:d * dm + 9 * s + 9, :]
                    acc = blk if acc is None else acc + blk
            return jnp.concatenate([acc[k:k + 1, :] for k in range(9)], axis=1)
        if nm == "pool_grp":
            return ptot_ref[...]
        row, lo, n = {"norm_ffn1": (0, 0, D), "norm_mix": (1, 0, D), "norm_ffn2": (2, 0, D), "pool_scale": (3, 0, POOL_W),
                      "q_a_norm": (4, 0, QL), "kv_a_norm": (5, 0, KVL), "q_norm_nope": (6, 0, NOPE),
                      "q_norm_rope": (6, NOPE, ROPE), "k_norm_nope": (7, 0, NOPE), "k_norm_rope": (7, KR_LANE, ROPE)}[nm]
        return red_ref[row:row + 1, lo:lo + n]

    def body(*refs):
        red_ref, dmod_ref, ptot_ref = refs[:3]
        ins = refs[3:3 + 3 * len(names)]
        outs = refs[3 + 3 * len(names):]
        outs[4 * len(names)][...] = red_ref[RED_ROWS - 1:RED_ROWS, LOSS_LANE:]
        for i, nm in enumerate(names):
            g = grad_of(nm, red_ref, dmod_ref, ptot_ref)
            d, mn, vn = _adamw_math(ins[3 * i][...], g, ins[3 * i + 1][...], ins[3 * i + 2][...])
            outs[4 * i][...] = g
            outs[4 * i + 1][...] = d
            outs[4 * i + 2][...] = mn
            outs[4 * i + 3][...] = vn

    flat_in = [a for nm in names for a in params[nm]]
    out_shape = [_sds(params[nm][0].shape, F32) for nm in names for _ in range(4)] + [_sds((1, LANE), F32)]
    res = _pcall(body, name=name, in_specs=[VMEM_SPEC] * (3 + len(flat_in)), out_specs=[VMEM_SPEC] * len(out_shape),
                 out_shape=out_shape)(red, dmod_all, pool_total, *flat_in)
    return {nm: tuple(res[4 * i:4 * i + 4]) for i, nm in enumerate(names)}, res[-1]


def _adamw(w, g, m, v, *, name, comm=None):
    r, c = w.shape
    tr = _pick_rows(r, 256)
    tc = c if tr < r else _pick(c, 256)
    spec = pl.BlockSpec((tr, tc), lambda i, j: (i, j))
    bc1 = 1.0 / (1.0 - ADAM_B1 ** ADAM_STEP)
    bc2 = 1.0 / (1.0 - ADAM_B2 ** ADAM_STEP)

    def body(w_ref, g_ref, m_ref, v_ref, d_ref, mo_ref, vo_ref):
        gv = g_ref[...]
        mn = ADAM_B1 * m_ref[...] + (1.0 - ADAM_B1) * gv
        vn = ADAM_B2 * v_ref[...] + (1.0 - ADAM_B2) * (gv * gv)
        mo_ref[...] = mn
        vo_ref[...] = vn
        d_ref[...] = -ADAM_LR * ((mn * bc1) / (jnp.sqrt(vn * bc2) + ADAM_EPS) + ADAM_WD * w_ref[...])

    out = _sds((r, c), F32)
    return _pcall(body, name=name, out_shape=[out, out, out], grid=(r // tr, c // tc), in_specs=[spec] * 4, out_specs=[spec] * 3,
                  comm=comm)(w, g, m, v)


BIG = ("w_ffn1_in", "w_ffn1_out", "w_in", "w_pool_proj", "w_q_up", "w_kv_up", "w_mla_proj", "w_out", "w_ffn2_in", "w_ffn2_out")
ROW_SHARDED = ("w_ffn1_out", "w_out", "w_ffn2_out")
KERNEL_NAME = {"w_ffn1_in": "ffn1_in", "w_ffn1_out": "ffn1_out", "w_in": "w_in", "w_pool_proj": "pool_proj", "w_q_up": "q_up",
               "w_kv_up": "kv_up", "w_mla_proj": "mla_proj", "w_out": "w_out", "w_ffn2_in": "ffn2_in", "w_ffn2_out": "ffn2_out"}
WEIGHTS = ("w_ada", "b_ada", "norm_ffn1", "w_ffn1_in", "w_ffn1_out", "norm_mix", "w_in", "pool_grp", "pool_scale", "w_pool_proj",
           "q_a_norm", "w_q_up", "kv_a_norm", "w_kv_up", "q_norm_nope", "q_norm_rope", "k_norm_nope", "k_norm_rope", "w_mla_proj",
           "w_out", "norm_ffn2", "w_ffn2_in", "w_ffn2_out")
SMALL = ("b_ada", "norm_ffn1", "norm_mix", "pool_grp", "pool_scale", "q_a_norm", "kv_a_norm", "q_norm_nope", "q_norm_rope",
         "k_norm_nope", "k_norm_rope", "norm_ffn2")


def _assemble(name, stacked):
    if name in ROW_SHARDED:
        return stacked.reshape(stacked.shape[0] * stacked.shape[1], stacked.shape[2])
    return jnp.transpose(stacked, (1, 0, 2)).reshape(stacked.shape[1], stacked.shape[0] * stacked.shape[2])


def _split(name, full):
    if name in ROW_SHARDED:
        return full.reshape(N_CHIP, full.shape[0] // N_CHIP, full.shape[1])
    return jnp.transpose(full.reshape(full.shape[0], N_CHIP, full.shape[1] // N_CHIP), (1, 0, 2))


GU = ("w_ffn1_in", "w_ffn2_in")
NARROW = ("w_in", "w_q_up")


MIX_SMALL = ("w_out", "w_pool_proj", "w_mla_proj", "w_q_up", "w_kv_up")
RIDES = {
    "ada_prologue": (("gather", ("w_ffn1_in",)),),
    "ffn1_up_a": (("gather", ("w_ffn1_out",)),),
    "ffn1_up_b": (("gather", ("w_in",)),),
    "mix_in": (("gather", MIX_SMALL),),
    "attn_fwd": (("gather", ("w_ffn2_in", "w_ffn2_out")),),
    "d_ffn2_h": (("swap", ("w_ffn2_out", "w_ffn2_in")),),
    "attn_bwd": (("exchange", ("w_ffn2_out", "w_ffn2_in")),),
    "d_mix_in": (("swap", MIX_SMALL),),
    "d_mix_h": (("exchange", MIX_SMALL), ("swap", ("w_in",))),
    "ffn1_dact": (("exchange", ("w_in",)), ("swap", ("w_ffn1_out",))),
    "d_ffn1_in_a": (("exchange", ("w_ffn1_out",)),),
    "d_ffn1_in_b": (("swap", ("w_ffn1_in@0",)),),
    "d_ffn1_h": (("exchange", ("w_ffn1_in@0",)), ("swap", ("w_ffn1_in@1",))),
    "bwd_norm1": (("exchange", ("w_ffn1_in@1",)),),
    "small_allreduce": (("join", tuple(n for n in BIG if n != "w_ffn1_in") + ("w_ffn1_in@0", "w_ffn1_in@1")),),
}


def _kname(n):
    base, _, part = n.partition("@")
    return KERNEL_NAME[base] + ("@" + part if part else "")


def _both(comms):
    if len(comms) == 1:
        return comms[0]
    ins, outs, sems, aliases, spans = [], [], [], {}, []
    for c in comms:
        spans.append((len(ins), len(c.ins), len(outs), len(c.out_shapes), len(sems), len(c.sems)))
        aliases.update({len(ins) + i: len(outs) + o for i, o in c.aliases.items()})
        ins, outs, sems = ins + c.ins, outs + c.out_shapes, sems + c.sems

    def each(which):
        def run(i_, o_, s_):
            for c, (ia, ni, oa, no, sa, ns) in zip(comms, spans):
                getattr(c, which)(i_[ia:ia + ni], o_[oa:oa + no], s_[sa:sa + ns])
        return run

    return _Comm(ins, outs, sems, each("start"), each("finish"), aliases)


class _ExchangePlan:
    def __init__(self, shards, cidx):
        self.shards, self.cidx = shards, cidx
        self.W, self.G, self.parts, self.pre, self.reduced, self.joined = {}, {}, {}, {}, {}, {}

    def grad(self, key, g):
        self.G[key] = g

    def rider(self, name):
        comms = []
        for kind, names in RIDES.get(name, ()):
            if kind == "gather":
                comms.append(_gather_comm([self.shards[n] for n in names]))
            elif kind == "swap":
                for n in names:
                    self.parts[n] = self._stacked(n)
                comms.append(_swap_comm([self.parts[n] for n in names]))
            elif kind == "exchange":
                comms.append(_exchange_comm([self.pre[n] for n in names]))
            else:
                comms.append(_join_comm([self.reduced[n] for n in names]))
        return _both(comms) if comms else None

    def landed(self, name, outs):
        at = 0
        for kind, names in RIDES[name]:
            for n, o in zip(names, outs[at:at + len(names)]):
                if kind == "gather":
                    self.W[KERNEL_NAME[n]] = self._to_kernel(n, o)
                elif kind == "swap":
                    self.pre[n] = _add_half(self.parts[n], o, self.cidx, name="rs_add_" + _kname(n))
                elif kind == "exchange":
                    self.reduced[n] = _sum_chips(o, self.cidx, name="rs_sum_" + _kname(n))
                else:
                    self.joined[n] = o
            at += len(names)

    @staticmethod
    def _to_kernel(n, stacked):
        if n in GU:
            return stacked
        full = _assemble(n, stacked)
        return {"w_in": _w_in_to_kernel, "w_q_up": _q_up_to_kernel, "w_kv_up": _kv_up_to_kernel}.get(n, lambda w: w)(full)

    def _stacked(self, n):
        g = self.G[_kname(n)]
        if n.partition("@")[0] in GU:
            return g
        full = {"w_in": _w_in_from_kernel, "w_q_up": _q_up_from_kernel, "w_kv_up": _kv_up_from_kernel}.get(n, lambda w: w)(g)
        return _split(n, full)


def kernel(x, c, positions, w_ada, b_ada, norm_ffn1, w_ffn1_in, w_ffn1_out, norm_mix, w_in, pool_grp, pool_scale, w_pool_proj, q_a_norm, w_q_up, kv_a_norm, w_kv_up, q_norm_nope, q_norm_rope, k_norm_nope, k_norm_rope, w_mla_proj, w_out, norm_ffn2, w_ffn2_in, w_ffn2_out, loss_target, m_w_ada, m_b_ada, m_norm_ffn1, m_w_ffn1_in, m_w_ffn1_out, m_norm_mix, m_w_in, m_pool_grp, m_pool_scale, m_w_pool_proj, m_q_a_norm, m_w_q_up, m_kv_a_norm, m_w_kv_up, m_q_norm_nope, m_q_norm_rope, m_k_norm_nope, m_k_norm_rope, m_w_mla_proj, m_w_out, m_norm_ffn2, m_w_ffn2_in, m_w_ffn2_out, v_w_ada, v_b_ada, v_norm_ffn1, v_w_ffn1_in, v_w_ffn1_out, v_norm_mix, v_w_in, v_pool_grp, v_pool_scale, v_w_pool_proj, v_q_a_norm, v_w_q_up, v_kv_a_norm, v_w_kv_up, v_q_norm_nope, v_q_norm_rope, v_k_norm_nope, v_k_norm_rope, v_w_mla_proj, v_w_out, v_norm_ffn2, v_w_ffn2_in, v_w_ffn2_out):
    args = dict(locals())
    wts = {n: args[n][0] for n in WEIGHTS}
    mom = {n: args["m_" + n][0] for n in WEIGHTS}
    var = {n: args["v_" + n][0] for n in WEIGHTS}
    nb, seq, dm = x.shape
    tokens = nb * seq
    xi, yi, ci = _coords()
    chip = 2 * xi + yi

    cidx = ci.astype(jnp.int32).reshape(1)
    plan = _ExchangePlan({n: wts[n].astype(MX) for n in BIG}, cidx)
    plan.W["pool_grp"] = wts["pool_grp"].astype(MX)

    ncol = w_ada.shape[2]
    b_cols = lax.dynamic_slice_in_dim(wts["b_ada"].reshape(1, -1), chip * ncol, ncol, axis=1)
    c_slots, mod_slots = _run(plan, _ada_prologue, c, wts["w_ada"], b_cols, name="ada_prologue")
    c_all = c_slots[:, :nb].reshape(N_DEV * nb, dm)
    mod3 = jnp.transpose(mod_slots[:, :nb], (1, 0, 2)).reshape(nb, 9, dm)
    P = {"norm_ffn1": wts["norm_ffn1"].reshape(1, dm), "norm_mix": wts["norm_mix"].reshape(1, dm),
         "norm_ffn2": wts["norm_ffn2"].reshape(1, dm), "pool_scale": wts["pool_scale"].reshape(1, POOL_W),
         "q_a_norm": wts["q_a_norm"].reshape(1, QL), "kv_a_norm": wts["kv_a_norm"].reshape(1, KVL),
         "q_gain": _gain_slab(wts["q_norm_nope"].reshape(1, NOPE), wts["q_norm_rope"].reshape(1, ROPE)),
         "k_gain": _gain_slab(wts["k_norm_nope"].reshape(1, NOPE), wts["k_norm_rope"].reshape(1, ROPE))}
    cos, sin = _rope_tables(positions.reshape(tokens))

    loss8, grad_x, stats, d_pool_grp = _layer_fwd_bwd(x.reshape(tokens, dm), loss_target.reshape(tokens, dm), mod3, cos, sin, plan, P)

    slab = _pack_stats(*stats, loss8, name="pack_stats")
    red, dmod_rows, pool_total = _run(plan, _small_allreduce, slab, d_pool_grp.reshape(POOL_G * LANE, LANE), name="small_allreduce")
    grads = {n: plan.joined[n] for n in BIG if n != "w_ffn1_in"}
    grads["w_ffn1_in"] = jnp.concatenate([plan.joined["w_ffn1_in@0"], plan.joined["w_ffn1_in@1"]], axis=0)
    dm_rows = dmod_rows.shape[0] // N_DEV
    dmod_all = dmod_rows.reshape(N_DEV, dm_rows, SLAB_W)[:, :9 * nb].reshape(N_DEV * nb, 9 * dm)
    dmod_cols = lax.dynamic_slice_in_dim(dmod_all, chip * ncol, ncol, axis=1)
    grads["w_ada"] = _ada_bwd(c_all, dmod_cols, name="ada_bwd")

    delta, new_m, new_v = {}, {}, {}
    as2d = lambda a: a.reshape(POOL_G * LANE, LANE) if a.ndim == 4 else a.reshape(1, -1)
    upd, loss_row = _small_update(red, dmod_rows, pool_total, nb,
                                  {n: tuple(as2d(args[p + n]) for p in ("", "m_", "v_")) for n in SMALL}, name="small_update")
    loss = loss_row[0, 0]
    for n in SMALL:
        grads[n], delta[n], new_m[n], new_v[n] = upd[n]
    for n in ("w_ada",) + BIG:
        if n in NARROW:
            res = _adamw(wts[n].T, grads[n].T, mom[n].T, var[n].T, name="adamw_" + n)
            delta[n], new_m[n], new_v[n] = (r.T for r in res)
        else:
            delta[n], new_m[n], new_v[n] = _adamw(wts[n], grads[n], mom[n], var[n], name="adamw_" + n)

    def lead(a, n):
        return a.reshape((1,) + wts[n].shape)

    return (loss, grad_x.reshape(nb, seq, dm), *[lead(grads[n], n) for n in WEIGHTS], *[lead(delta[n], n) for n in WEIGHTS],
            *[lead(new_m[n], n) for n in WEIGHTS], *[lead(new_v[n], n) for n in WEIGHTS])
```

```python
import functools
import math

import jax
import jax.numpy as jnp
from jax import lax
from jax.experimental import pallas as pl
from jax.experimental.pallas import tpu as pltpu

F32 = jnp.float32
MX = jnp.bfloat16

D = 1024
DFF = 2816
NH = 8
POOL_W = 512
POOL_G = 4
QL = 384
KVL = 256
ROPE = 32
NOPE = 64
LANE = 128
SUBLANE = 8
EPS = 1e-6
ATTN_SCALE = 1.0 / math.sqrt(96.0)
NEG = -1e30

Z_UP, Z_QL, Z_KV, Z_KR, Z_GP, Z_GM, Z_W = 0, 512, 896, 1152, 1280, 2304, 3328
KR_LANE = 64
LAT_W = 1280

ADAM_LR, ADAM_B1, ADAM_B2, ADAM_EPS, ADAM_WD, ADAM_STEP = 0.001, 0.9, 0.999, 1e-08, 0.01, 10

VMEM_LIMIT = 48 * 1024 * 1024
MESH = pl.DeviceIdType.MESH
N_DEV = 8
N_CHIP = 4


class _Comm:
    def __init__(self, ins, out_shapes, sems, start, finish, aliases=None):
        self.ins, self.out_shapes, self.sems = list(ins), list(out_shapes), list(sems)
        self.start, self.finish = start, finish
        self.aliases = aliases or {}


def _pcall(body, *, name, out_shape, grid=(), in_specs=None, out_specs=None, scratch=(), grid_spec=None, aliases=None,
           comm=None):
    params = pltpu.CompilerParams(vmem_limit_bytes=VMEM_LIMIT)
    kw = dict(name=name, compiler_params=params)
    if comm is None:
        if aliases:
            kw["input_output_aliases"] = aliases
        if grid_spec is not None:
            return pl.pallas_call(body, grid_spec=grid_spec, out_shape=out_shape, **kw)
        return pl.pallas_call(body, grid=grid, in_specs=in_specs, out_specs=out_specs, scratch_shapes=scratch,
                              out_shape=out_shape, **kw)
    single = not isinstance(out_shape, (list, tuple))
    outs = [out_shape] if single else list(out_shape)
    ospecs = [out_specs] if single else list(out_specs)
    n_in, n_out, n_ci, n_co, n_scr = len(in_specs), len(outs), len(comm.ins), len(comm.out_shapes), len(scratch)
    io = dict(aliases or {})
    io.update({n_in + i: n_out + o for i, o in comm.aliases.items()})

    def riding(*refs):
        ins, cins = refs[:n_in], refs[n_in:n_in + n_ci]
        at = n_in + n_ci
        os_, couts = refs[at:at + n_out], refs[at + n_out:at + n_out + n_co]
        at += n_out + n_co
        scr, csems = refs[at:at + n_scr], refs[at + n_scr:]
        if grid:
            first = functools.reduce(jnp.logical_and, [pl.program_id(d) == 0 for d in range(len(grid))])
            last = functools.reduce(jnp.logical_and, [pl.program_id(d) == grid[d] - 1 for d in range(len(grid))])
            pl.when(first)(lambda: comm.start(cins, couts, csems))
            body(*ins, *os_, *scr)
            pl.when(last)(lambda: comm.finish(cins, couts, csems))
        else:
            comm.start(cins, couts, csems)
            body(*ins, *os_, *scr)
            comm.finish(cins, couts, csems)

    call = pl.pallas_call(riding, grid=grid, in_specs=list(in_specs) + [HBM_SPEC] * n_ci, out_specs=ospecs + [HBM_SPEC] * n_co,
                          out_shape=outs + comm.out_shapes, scratch_shapes=list(scratch) + comm.sems,
                          input_output_aliases=io, **kw)

    def run(*args):
        res = call(*args, *comm.ins)
        main = list(res[:n_out])
        return (main[0] if single else main), list(res[n_out:])

    return run


def _pick(dim, target):
    best = None
    for t in range(LANE, min(dim, target) + 1, LANE):
        if dim % t == 0:
            best = t
    return dim if best is None else best


def _sds(shape, dtype):
    return jax.ShapeDtypeStruct(shape, dtype)


def _dw(a, g, *, name, tm=512, tn=1024, out_t=False, comm=None):
    return _mm(a, g, mode="tn", out_dtype=F32, name=name, tm=tm, tn=tn, tk=a.shape[0], n_outer=True, out_t=out_t, comm=comm)


def _mm(a, b, *, mode, out_dtype, name, tm=1024, tn=1024, tk=4096, n_outer=False, out_t=False, comm=None):
    if mode == "nn":
        (M, K), (K2, N) = a.shape, b.shape
    elif mode == "nt":
        (M, K), (N, K2) = a.shape, b.shape
    else:
        (K, M), (K2, N) = a.shape, b.shape
    assert K == K2, (name, a.shape, b.shape)
    tm, tn, tk = _pick(M, tm), _pick(N, tn), _pick(K, tk)
    nk = K // tk
    if n_outer:
        ij = lambda g0, g1: (g1, g0)
        grid = (N // tn, M // tm, nk)
    else:
        ij = lambda g0, g1: (g0, g1)
        grid = (M // tm, N // tn, nk)
    if mode == "tn":
        a_spec = pl.BlockSpec((tk, tm), lambda g0, g1, k: (k, ij(g0, g1)[0]))
    else:
        a_spec = pl.BlockSpec((tm, tk), lambda g0, g1, k: (ij(g0, g1)[0], k))
    if mode == "nt":
        b_spec = pl.BlockSpec((tn, tk), lambda g0, g1, k: (ij(g0, g1)[1], k))
    else:
        b_spec = pl.BlockSpec((tk, tn), lambda g0, g1, k: (k, ij(g0, g1)[1]))
    if out_t:
        assert nk == 1, name
        o_spec = pl.BlockSpec((tn, tm), lambda g0, g1, k: ij(g0, g1)[::-1])
    else:
        o_spec = pl.BlockSpec((tm, tn), lambda g0, g1, k: ij(g0, g1))
    dn = {"nn": (((1,), (0,)), ((), ())), "nt": (((1,), (1,)), ((), ())), "tn": (((0,), (0,)), ((), ()))}[mode]

    def dot(a_ref, b_ref):
        return lax.dot_general(a_ref[...].astype(MX), b_ref[...].astype(MX), dn, preferred_element_type=F32)

    def body_one(a_ref, b_ref, o_ref):
        prod = dot(a_ref, b_ref)
        o_ref[...] = (prod.T if out_t else prod).astype(o_ref.dtype)

    def body_acc(a_ref, b_ref, o_ref, acc_ref):
        k = pl.program_id(2)
        part = dot(a_ref, b_ref)

        @pl.when(k == 0)
        def _():
            acc_ref[...] = part

        @pl.when(k > 0)
        def _():
            acc_ref[...] += part

        @pl.when(k == nk - 1)
        def _():
            o_ref[...] = acc_ref[...].astype(o_ref.dtype)

    return _pcall(body_one if nk == 1 else body_acc, name=name, out_shape=_sds((N, M) if out_t else (M, N), out_dtype), grid=grid,
                  in_specs=[a_spec, b_spec], out_specs=o_spec, scratch=[] if nk == 1 else [pltpu.VMEM((tm, tn), F32)],
                  comm=comm)(a, b)


def _gu_shard(q):
    return (q % 2) * 2 + q // 2


def _ffn_up(h, w_st, *, name, tm=512, comm=None):
    T, dm = h.shape
    hw = w_st.shape[2]
    tm = _pick(T, tm)

    def body(h_ref, wg_ref, wu_ref, gu_ref, a_ref):
        hv = h_ref[...]
        g = jnp.dot(hv, wg_ref[0], preferred_element_type=F32)
        u = jnp.dot(hv, wu_ref[0], preferred_element_type=F32)
        gu_ref[:, :hw] = g.astype(gu_ref.dtype)
        gu_ref[:, hw:] = u.astype(gu_ref.dtype)
        a_ref[...] = (g * _sigmoid(g) * u).astype(a_ref.dtype)

    return _pcall(body, name=name, grid=(T // tm, 2),
                  in_specs=[pl.BlockSpec((tm, dm), lambda i, j: (i, 0)), pl.BlockSpec((1, dm, hw), lambda i, j: (j, 0, 0)),
                            pl.BlockSpec((1, dm, hw), lambda i, j: (2 + j, 0, 0))],
                  out_specs=[pl.BlockSpec((tm, 2 * hw), lambda i, j: (i, j)), pl.BlockSpec((tm, hw), lambda i, j: (i, j))],
                  out_shape=[_sds((T, 4 * hw), MX), _sds((T, 2 * hw), MX)], comm=comm)(h, w_st, w_st)


def _ffn_up_first(x, mod3, gain, w_st, *, sub, name, tm=512, comm=None):
    T, dm = x.shape
    hw = w_st.shape[2]
    tm = _pick(T, tm)
    tps = (T // mod3.shape[0]) // tm

    def body(x_ref, mod_ref, n_ref, wg_ref, wu_ref, h_ref, gu_ref, a_ref):
        xv = x_ref[...]
        xn = xv * _rsq(xv) * n_ref[...]
        hv = (xn * (1.0 + mod_ref[0, 3 * sub + 1:3 * sub + 2, :]) + mod_ref[0, 3 * sub:3 * sub + 1, :]).astype(h_ref.dtype)
        h_ref[...] = hv
        g = jnp.dot(hv, wg_ref[0], preferred_element_type=F32)
        u = jnp.dot(hv, wu_ref[0], preferred_element_type=F32)
        gu_ref[:, :hw] = g.astype(gu_ref.dtype)
        gu_ref[:, hw:] = u.astype(gu_ref.dtype)
        a_ref[...] = (g * _sigmoid(g) * u).astype(a_ref.dtype)

    return _pcall(body, name=name, grid=(T // tm,),
                  in_specs=[_row_spec(tm, dm), pl.BlockSpec((1, 9, dm), lambda i: (i // tps, 0, 0)),
                            pl.BlockSpec((1, dm), lambda i: (0, 0)), pl.BlockSpec((1, dm, hw), lambda i: (0, 0, 0)),
                            pl.BlockSpec((1, dm, hw), lambda i: (2, 0, 0))],
                  out_specs=[_row_spec(tm, dm), pl.BlockSpec((tm, 2 * hw), lambda i: (i, 0)), pl.BlockSpec((tm, hw), lambda i: (i, 0))],
                  out_shape=[_sds((T, dm), MX), _sds((T, 4 * hw), MX), _sds((T, 2 * hw), MX)],
                  comm=comm)(x, mod3, gain, w_st, w_st)


def _ffn_up_second(h, w_st, gu, a, *, name, tm=512, comm=None):
    T, dm = h.shape
    hw = w_st.shape[2]
    tm = _pick(T, tm)

    def body(h_ref, wg_ref, wu_ref, gu_in, a_in, gu_ref, a_ref):
        hv = h_ref[...]
        g = jnp.dot(hv, wg_ref[0], preferred_element_type=F32)
        u = jnp.dot(hv, wu_ref[0], preferred_element_type=F32)
        gu_ref[:, :hw] = g.astype(gu_ref.dtype)
        gu_ref[:, hw:] = u.astype(gu_ref.dtype)
        a_ref[...] = (g * _sigmoid(g) * u).astype(a_ref.dtype)

    return _pcall(body, name=name, grid=(T // tm,),
                  in_specs=[_row_spec(tm, dm), pl.BlockSpec((1, dm, hw), lambda i: (1, 0, 0)),
                            pl.BlockSpec((1, dm, hw), lambda i: (3, 0, 0)), HBM_SPEC, HBM_SPEC],
                  out_specs=[pl.BlockSpec((tm, 2 * hw), lambda i: (i, 1)), pl.BlockSpec((tm, hw), lambda i: (i, 1))],
                  out_shape=[_sds(gu.shape, gu.dtype), _sds(a.shape, a.dtype)], aliases={3: 0, 4: 1},
                  comm=comm)(h, w_st, w_st, gu, a)


def _ffn_dact(df, gu, w_out, *, name, tm=512, comm=None):
    T, dm = df.shape
    hw = gu.shape[1] // 4
    tm = _pick(T, tm)

    def body(df_ref, gu_ref, wo_ref, dgu_ref):
        da = lax.dot_general(df_ref[...], wo_ref[...], (((1,), (1,)), ((), ())), preferred_element_type=F32)
        g = gu_ref[:, :hw].astype(F32)
        u = gu_ref[:, hw:].astype(F32)
        s = _sigmoid(g)
        dgu_ref[:, :hw] = (da * u * (s * (1.0 + g * (1.0 - s)))).astype(dgu_ref.dtype)
        dgu_ref[:, hw:] = (da * (g * s)).astype(dgu_ref.dtype)

    return _pcall(body, name=name, grid=(T // tm, 2),
                  in_specs=[pl.BlockSpec((tm, dm), lambda i, j: (i, 0)), pl.BlockSpec((tm, 2 * hw), lambda i, j: (i, j)),
                            pl.BlockSpec((hw, dm), lambda i, j: (j, 0))],
                  out_specs=pl.BlockSpec((tm, 2 * hw), lambda i, j: (i, j)), out_shape=_sds(gu.shape, MX),
                  comm=comm)(df, gu, w_out)


def _ffn_dh(dgu, w_st, *, name, tm=1024, comm=None):
    T = dgu.shape[0]
    _, dm, hw = w_st.shape
    tm = _pick(T, tm)

    def body(d_ref, w_ref, o_ref, acc_ref):
        q = pl.program_id(1)
        part = lax.dot_general(d_ref[...], w_ref[0], (((1,), (1,)), ((), ())), preferred_element_type=F32)

        @pl.when(q == 0)
        def _():
            acc_ref[...] = part

        @pl.when(jnp.logical_and(q > 0, q < 3))
        def _():
            acc_ref[...] += part

        @pl.when(q == 3)
        def _():
            o_ref[...] = acc_ref[...] + part

    return _pcall(body, name=name, grid=(T // tm, 4),
                  in_specs=[pl.BlockSpec((tm, hw), lambda i, q: (i, q)), pl.BlockSpec((1, dm, hw), lambda i, q: (_gu_shard(q), 0, 0))],
                  out_specs=pl.BlockSpec((tm, dm), lambda i, q: (i, 0)), out_shape=_sds((T, dm), F32),
                  scratch=[pltpu.VMEM((tm, dm), F32)], comm=comm)(dgu, w_st)


def _ffn_dw_in(h, dgu, *, name, rows=None, tm=512, comm=None):
    T, dm = h.shape
    hw = dgu.shape[1] // 4
    first, count = rows if rows is not None else (0, dm)
    tm = _pick(count, tm)
    skip = first // tm

    def body(h_ref, d_ref, o_ref):
        o_ref[0] = lax.dot_general(h_ref[...], d_ref[...], (((0,), (0,)), ((), ())), preferred_element_type=F32)

    return _pcall(body, name=name, grid=(4, count // tm),
                  in_specs=[pl.BlockSpec((T, tm), lambda q, i: (0, skip + i)), pl.BlockSpec((T, hw), lambda q, i: (0, q))],
                  out_specs=pl.BlockSpec((1, tm, hw), lambda q, i: (_gu_shard(q), i, 0)),
                  out_shape=_sds((4, count, hw), F32), comm=comm)(h, dgu)


def _rsq(x):
    return lax.rsqrt(jnp.mean(x * x, axis=-1, keepdims=True) + EPS)


def _sigmoid(x):
    return 1.0 / (1.0 + jnp.exp(-x))


def _row_spec(tm, w):
    return pl.BlockSpec((tm, w), lambda i: (i, 0))


def _mm_resid_norm(a, w, x_prev, mod3, gain, *, sub, coef, name, tm=512, comm=None):
    T, k = a.shape
    dm = w.shape[1]
    tm = _pick(T, tm)
    tps = (T // mod3.shape[0]) // tm

    def body(a_ref, w_ref, x_ref, mod_ref, n_ref, f_ref, xo_ref, h_ref):
        f = jnp.dot(a_ref[...], w_ref[...], preferred_element_type=F32)
        f_ref[...] = f
        x = x_ref[...] + coef * mod_ref[0, 3 * sub - 1:3 * sub, :] * f
        xo_ref[...] = x
        xn = x * _rsq(x) * n_ref[...]
        h_ref[...] = (xn * (1.0 + mod_ref[0, 3 * sub + 1:3 * sub + 2, :]) + mod_ref[0, 3 * sub:3 * sub + 1, :]).astype(h_ref.dtype)

    row = _row_spec(tm, dm)
    return _pcall(body, name=name, grid=(T // tm,),
                  in_specs=[_row_spec(tm, k), pl.BlockSpec((k, dm), lambda i: (0, 0)), row,
                            pl.BlockSpec((1, 9, dm), lambda i: (i // tps, 0, 0)), pl.BlockSpec((1, dm), lambda i: (0, 0))],
                  out_specs=[row, row, row], out_shape=[_sds((T, dm), F32), _sds((T, dm), F32), _sds((T, dm), MX)],
                  comm=comm)(a, w, x_prev, mod3, gain)


def _mm_loss(a, w, x2, tgt, mod3, *, name, tm=512):
    T, k = a.shape
    dm = w.shape[1]
    tm = _pick(T, tm)
    tps = (T // mod3.shape[0]) // tm
    mod_spec = pl.BlockSpec((1, 9, dm), lambda i: (i // tps, 0, 0))
    row = _row_spec(tm, dm)
    stat_spec = pl.BlockSpec((1, SUBLANE, dm), lambda i: (i // tps, 0, 0))
    loss_spec = pl.BlockSpec((SUBLANE, LANE), lambda i: (0, 0))

    def body(a_ref, w_ref, x_ref, t_ref, mod_ref, dy_ref, df_ref, st_ref, loss_ref):
        i = pl.program_id(0)
        g = mod_ref[0, 8:9, :]
        f = jnp.dot(a_ref[...], w_ref[...], preferred_element_type=F32)
        err = x_ref[...] + 0.5 * g * f - t_ref[...]
        dy = err * (1.0 / dm)
        dy_ref[...] = dy
        df_ref[...] = (0.5 * g * dy).astype(df_ref.dtype)
        dgate = jnp.sum(0.5 * dy * f, axis=0, keepdims=True)
        part = 0.5 * jnp.sum(jnp.sum(err * err, axis=0, keepdims=True), axis=1, keepdims=True) * (1.0 / dm)

        @pl.when(i % tps == 0)
        def _():
            st_ref[...] = jnp.zeros_like(st_ref)

        @pl.when(i == 0)
        def _():
            loss_ref[...] = jnp.zeros_like(loss_ref)

        st_ref[0, 0:1, :] += dgate
        loss_ref[...] += jnp.broadcast_to(part, loss_ref.shape)

    return _pcall(body, name=name, grid=(T // tm,),
                  in_specs=[_row_spec(tm, k), pl.BlockSpec((k, dm), lambda i: (0, 0)), row, row, mod_spec],
                  out_specs=[row, row, stat_spec, loss_spec],
                  out_shape=[_sds((T, dm), F32), _sds((T, dm), MX), _sds((mod3.shape[0], SUBLANE, dm), F32),
                             _sds((SUBLANE, LANE), F32)])(a, w, x2, tgt, mod3)


def _norm_bwd_tail(dhv, x_ref, dxi_ref, f_ref, mod_ref, n_ref, dx_ref, df_ref, st_ref, *, first, sub, coef):
    x = x_ref[...]
    r = _rsq(x)
    xhat = x * r
    n = n_ref[...]
    d_shift = jnp.sum(dhv, axis=0, keepdims=True)
    d_scale = jnp.sum(dhv * (xhat * n), axis=0, keepdims=True)
    dxn = dhv * (1.0 + mod_ref[0, 3 * sub + 1:3 * sub + 2, :])
    d_gain = jnp.sum(dxn * xhat, axis=0, keepdims=True)
    dxhat = dxn * n
    dx = dxi_ref[...] + r * (dxhat - xhat * jnp.mean(dxhat * xhat, axis=-1, keepdims=True))
    dx_ref[...] = dx

    @pl.when(first)
    def _():
        st_ref[...] = jnp.zeros_like(st_ref)

    st_ref[0, 0:1, :] += d_shift
    st_ref[0, 1:2, :] += d_scale
    st_ref[0, 2:3, :] += d_gain
    if f_ref is not None:
        st_ref[0, 3:4, :] += jnp.sum(coef * dx * f_ref[...], axis=0, keepdims=True)
        df_ref[...] = (coef * mod_ref[0, 3 * sub - 1:3 * sub, :] * dx).astype(df_ref.dtype)


def _ffn_dh_norm(dgu, w_st, x_cur, dx_in, f_prev, mod3, gain, *, sub, coef, name, tm=512, comm=None):
    T = dgu.shape[0]
    _, dm, hw = w_st.shape
    tm = _pick(T, tm)
    nb = mod3.shape[0]
    tps = (T // nb) // tm

    def body(d_ref, w_ref, x_ref, dxi_ref, f_ref, mod_ref, n_ref, dx_ref, df_ref, st_ref, acc_ref):
        i = pl.program_id(0)
        q = pl.program_id(1)
        part = lax.dot_general(d_ref[...], w_ref[0], (((1,), (1,)), ((), ())), preferred_element_type=F32)

        @pl.when(q == 0)
        def _():
            acc_ref[...] = part

        @pl.when(q > 0)
        def _():
            acc_ref[...] += part

        @pl.when(q == 3)
        def _():
            _norm_bwd_tail(acc_ref[...], x_ref, dxi_ref, f_ref, mod_ref, n_ref, dx_ref, df_ref, st_ref,
                           first=i % tps == 0, sub=sub, coef=coef)

    row = pl.BlockSpec((tm, dm), lambda i, q: (i, 0))
    return _pcall(body, name=name, grid=(T // tm, 4),
                  in_specs=[pl.BlockSpec((tm, hw), lambda i, q: (i, q)), pl.BlockSpec((1, dm, hw), lambda i, q: (_gu_shard(q), 0, 0)),
                            row, row, row, pl.BlockSpec((1, 9, dm), lambda i, q: (i // tps, 0, 0)),
                            pl.BlockSpec((1, dm), lambda i, q: (0, 0))],
                  out_specs=[row, row, pl.BlockSpec((1, SUBLANE, dm), lambda i, q: (i // tps, 0, 0))],
                  out_shape=[_sds((T, dm), F32), _sds((T, dm), MX), _sds((nb, SUBLANE, dm), F32)],
                  scratch=[pltpu.VMEM((tm, dm), F32)], comm=comm)(dgu, w_st, x_cur, dx_in, f_prev, mod3, gain)


def _mm_norm_bwd(a, b, x_cur, dx_in, f_prev, mod3, gain, *, sub, coef, name, tm=512, comm=None):
    T, k = a.shape
    dm = b.shape[1]
    tm = _pick(T, tm)
    nb = mod3.shape[0]
    tps = (T // nb) // tm

    def body(a_ref, b_ref, x_ref, dxi_ref, f_ref, mod_ref, n_ref, dx_ref, df_ref, st_ref):
        dh = jnp.dot(a_ref[...], b_ref[...], preferred_element_type=F32)
        _norm_bwd_tail(dh, x_ref, dxi_ref, f_ref, mod_ref, n_ref, dx_ref, df_ref, st_ref,
                       first=pl.program_id(0) % tps == 0, sub=sub, coef=coef)

    row = _row_spec(tm, dm)
    return _pcall(body, name=name, grid=(T // tm,),
                  in_specs=[_row_spec(tm, k), pl.BlockSpec((k, dm), lambda i: (0, 0)), row, row, row,
                            pl.BlockSpec((1, 9, dm), lambda i: (i // tps, 0, 0)), pl.BlockSpec((1, dm), lambda i: (0, 0))],
                  out_specs=[row, row, pl.BlockSpec((1, SUBLANE, dm), lambda i: (i // tps, 0, 0))],
                  out_shape=[_sds((T, dm), F32), _sds((T, dm), MX), _sds((nb, SUBLANE, dm), F32)],
                  comm=comm)(a, b, x_cur, dx_in, f_prev, mod3, gain)


def _bwd_block(x_cur, dh, dx_in, f_prev, mod3, gain, *, sub, coef, name, tm=256, comm=None):
    T, dm = x_cur.shape
    nb = mod3.shape[0]
    tps = (T // nb) // tm
    has_f = f_prev is not None
    mod_spec = pl.BlockSpec((1, 9, dm), lambda i: (i // tps, 0, 0))
    vec_spec = pl.BlockSpec((1, dm), lambda i: (0, 0))
    stat_spec = pl.BlockSpec((1, SUBLANE, dm), lambda i: (i // tps, 0, 0))
    row = _row_spec(tm, dm)

    def body(*refs):
        if has_f:
            x_ref, dh_ref, dxi_ref, f_ref, mod_ref, n_ref, dx_ref, df_ref, st_ref = refs
        else:
            x_ref, dh_ref, dxi_ref, mod_ref, n_ref, dx_ref, st_ref = refs
            f_ref = df_ref = None
        _norm_bwd_tail(dh_ref[...], x_ref, dxi_ref, f_ref, mod_ref, n_ref, dx_ref, df_ref, st_ref,
                       first=pl.program_id(0) % tps == 0, sub=sub, coef=coef)

    st_shape = _sds((nb, SUBLANE, dm), F32)
    if has_f:
        return _pcall(body, name=name, grid=(T // tm,), in_specs=[row, row, row, row, mod_spec, vec_spec],
                      out_specs=[row, row, stat_spec],
                      out_shape=[_sds((T, dm), F32), _sds((T, dm), MX), st_shape], comm=comm)(x_cur, dh, dx_in, f_prev, mod3, gain)
    return _pcall(body, name=name, grid=(T // tm,), in_specs=[row, row, row, mod_spec, vec_spec],
                  out_specs=[row, stat_spec], out_shape=[_sds((T, dm), F32), st_shape], comm=comm)(x_cur, dh, dx_in, mod3, gain)


def _mix_out_fwd(z, br_pool, attn, w_mla, w_out, x_prev, mod3, gain, *, sub, name, tm=512):
    T = z.shape[0]
    dm = w_out.shape[1]
    tm = _pick(T, tm)
    tps = (T // mod3.shape[0]) // tm
    whole = lambda a: pl.BlockSpec(a.shape, lambda i: (0, 0))

    def body(z_ref, bp_ref, at_ref, wm_ref, wo_ref, x_ref, mod_ref, n_ref, bm_ref, mg_ref, mo_ref, xo_ref, h_ref):
        bm = jnp.dot(at_ref[...], wm_ref[...], preferred_element_type=F32).astype(MX)
        bm_ref[...] = bm
        gp = z_ref[:, Z_GP:Z_GP + D].astype(F32)
        gm = z_ref[:, Z_GM:Z_GM + D].astype(F32)
        merged = (_sigmoid(gp) * bp_ref[...] + _sigmoid(gm) * bm).astype(MX)
        mg_ref[...] = merged
        mo = jnp.dot(merged, wo_ref[...], preferred_element_type=F32)
        mo_ref[...] = mo
        x = x_ref[...] + mod_ref[0, 3 * sub - 1:3 * sub, :] * mo
        xo_ref[...] = x
        xn = x * _rsq(x) * n_ref[...]
        h_ref[...] = (xn * (1.0 + mod_ref[0, 3 * sub + 1:3 * sub + 2, :]) + mod_ref[0, 3 * sub:3 * sub + 1, :]).astype(h_ref.dtype)

    row = _row_spec(tm, dm)
    return _pcall(body, name=name, grid=(T // tm,),
                  in_specs=[_row_spec(tm, Z_W), row, _row_spec(tm, attn.shape[1]), whole(w_mla), whole(w_out), row,
                            pl.BlockSpec((1, 9, dm), lambda i: (i // tps, 0, 0)), pl.BlockSpec((1, dm), lambda i: (0, 0))],
                  out_specs=[row, row, row, row, row],
                  out_shape=[_sds((T, dm), MX), _sds((T, dm), MX), _sds((T, dm), F32), _sds((T, dm), F32), _sds((T, dm), MX)],
                  )(z, br_pool, attn, w_mla, w_out, x_prev, mod3, gain)


def _mix_out_bwd(dmo, w_out, z, br_pool, br_mla, w_mla, *, name, tm=512):
    T = z.shape[0]
    tm = _pick(T, tm)
    whole = lambda a: pl.BlockSpec(a.shape, lambda i: (0, 0))
    nt = (((1,), (1,)), ((), ()))

    def body(dmo_ref, wo_ref, z_ref, bp_ref, bm_ref, wm_ref, dbp_ref, dbm_ref, dg_ref, dat_ref):
        dm = lax.dot_general(dmo_ref[...], wo_ref[...], nt, preferred_element_type=F32)
        sp = _sigmoid(z_ref[:, Z_GP:Z_GP + D].astype(F32))
        sm = _sigmoid(z_ref[:, Z_GM:Z_GM + D].astype(F32))
        dbp_ref[...] = (dm * sp).astype(dbp_ref.dtype)
        dbm = (dm * sm).astype(MX)
        dbm_ref[...] = dbm
        dg_ref[:, :D] = (dm * bp_ref[...].astype(F32) * sp * (1.0 - sp)).astype(dg_ref.dtype)
        dg_ref[:, D:] = (dm * bm_ref[...].astype(F32) * sm * (1.0 - sm)).astype(dg_ref.dtype)
        dat_ref[...] = lax.dot_general(dbm, wm_ref[...], nt, preferred_element_type=F32)

    row = _row_spec(tm, D)
    return _pcall(body, name=name, grid=(T // tm,),
                  in_specs=[row, whole(w_out), _row_spec(tm, Z_W), row, row, whole(w_mla)],
                  out_specs=[row, row, _row_spec(tm, 2 * D), _row_spec(tm, w_mla.shape[0])],
                  out_shape=[_sds((T, D), MX), _sds((T, D), MX), _sds((T, 2 * D), MX), _sds((T, w_mla.shape[0]), F32)],
                  )(dmo, w_out, z, br_pool, br_mla, w_mla)


def _shift_down(x, k, row):
    return jnp.where(row >= k, pltpu.roll(x, k, 0), 0.0)


def _shift_up(x, k, row, n):
    return jnp.where(row < n - k, pltpu.roll(x, n - k, 0), 0.0)


def _pool_fwd(z, pool_grp, pool_scale, *, nb, name):
    T = z.shape[0]
    S = T // nb
    blk = pl.BlockSpec((S, LANE), lambda b, g: (b, g))

    def body(u_ref, w_ref, s_ref, pooled_ref, mixed_ref, scaled_ref):
        g = pl.program_id(1)
        u = u_ref[...].astype(F32)
        row = lax.broadcasted_iota(jnp.int32, u.shape, 0)
        s2 = u + _shift_down(u, 1, row)
        s4 = s2 + _shift_down(s2, 2, row)
        s8 = s4 + _shift_down(s4, 4, row)
        s16 = s8 + _shift_down(s8, 8, row)
        win = jnp.where(g == 0, s2, jnp.where(g == 1, s4, jnp.where(g == 2, s8, s16)))
        width = lax.shift_left(jnp.int32(2), g)
        cnt = jnp.minimum(row + 1, width).astype(F32)
        pooled = (win / cnt - u).astype(MX)
        pooled_ref[...] = pooled
        mixed = jnp.dot(pooled, w_ref[0], preferred_element_type=F32)
        mixed_ref[...] = mixed
        scaled_ref[...] = (mixed * s_ref[...]).astype(scaled_ref.dtype)

    return _pcall(body, name=name, grid=(nb, POOL_G),
                  in_specs=[blk, pl.BlockSpec((1, LANE, LANE), lambda b, g: (g, 0, 0)),
                            pl.BlockSpec((1, LANE), lambda b, g: (0, g))],
                  out_specs=[blk, blk, blk],
                  out_shape=[_sds((T, POOL_W), MX), _sds((T, POOL_W), F32), _sds((T, POOL_W), MX)])(z, pool_grp, pool_scale)


def _pool_bwd(dscaled, mixed, pooled, pool_grp, pool_scale, *, nb, name):
    T = dscaled.shape[0]
    S = T // nb
    blk = pl.BlockSpec((S, LANE), lambda g, b: (b, g))

    def body(ds_ref, mixed_ref, pooled_ref, w_ref, s_ref, du_ref, dw_ref, dsc_ref):
        g = pl.program_id(0)
        b = pl.program_id(1)
        ds = ds_ref[...]
        dsc_ref[0] = jnp.sum(ds * mixed_ref[...], axis=0, keepdims=True)
        dmixed = (ds * s_ref[...]).astype(MX)
        dw = lax.dot_general(pooled_ref[...], dmixed, (((0,), (0,)), ((), ())), preferred_element_type=F32)

        @pl.when(b == 0)
        def _():
            dw_ref[0] = dw

        @pl.when(b > 0)
        def _():
            dw_ref[0] += dw
        dpooled = lax.dot_general(dmixed, w_ref[0], (((1,), (1,)), ((), ())), preferred_element_type=F32)
        row = lax.broadcasted_iota(jnp.int32, dpooled.shape, 0)
        width = lax.shift_left(jnp.int32(2), g)
        q = dpooled / jnp.minimum(row + 1, width).astype(F32)
        r2 = q + _shift_up(q, 1, row, S)
        r4 = r2 + _shift_up(r2, 2, row, S)
        r8 = r4 + _shift_up(r4, 4, row, S)
        r16 = r8 + _shift_up(r8, 8, row, S)
        win = jnp.where(g == 0, r2, jnp.where(g == 1, r4, jnp.where(g == 2, r8, r16)))
        du_ref[...] = (win - dpooled).astype(du_ref.dtype)

    return _pcall(body, name=name, grid=(POOL_G, nb),
                  in_specs=[blk, blk, blk, pl.BlockSpec((1, LANE, LANE), lambda g, b: (g, 0, 0)),
                            pl.BlockSpec((1, LANE), lambda g, b: (0, g))],
                  out_specs=[blk, pl.BlockSpec((1, LANE, LANE), lambda g, b: (g, 0, 0)),
                             pl.BlockSpec((1, 1, LANE), lambda g, b: (b, 0, g))],
                  out_shape=[_sds((T, POOL_W), MX), _sds((POOL_G, LANE, LANE), F32), _sds((nb, 1, POOL_W), F32)],
                  )(dscaled, mixed, pooled, pool_grp, pool_scale)


def _lane_masks(shape):
    lane = lax.broadcasted_iota(jnp.int32, shape, len(shape) - 1)
    m_n = lane < NOPE
    m_r = jnp.logical_and(lane >= KR_LANE, lane < KR_LANE + ROPE)
    first_half = lane < KR_LANE + ROPE // 2
    return m_n, m_r, first_half


def _rot(y, first_half):
    return jnp.where(first_half, -pltpu.roll(y, LANE - ROPE // 2, 1), pltpu.roll(y, ROPE // 2, 1))


def _rot_t(v, first_half, m_r):
    return jnp.where(m_r, jnp.where(first_half, pltpu.roll(v, LANE - ROPE // 2, 1), -pltpu.roll(v, ROPE // 2, 1)), 0.0)


def _mla_in_fwd(z, w_q, w_kv, qa_gain, kva_gain, cos, sin, q_gain, k_gain, *, name, tm=256):
    T = z.shape[0]
    slab = pl.BlockSpec((tm, LANE), lambda i: (i, 0))
    vec = pl.BlockSpec((1, LANE), lambda i: (0, 0))
    whole = lambda a: pl.BlockSpec(a.shape, lambda i: (0, 0))

    def body(z_ref, wq_ref, wkv_ref, qa_ref, kva_ref, cos_ref, sin_ref, qg_ref, kg_ref,
             qn_ref, kvn_ref, qp_ref, kvp_ref, q_ref, k_ref, v_ref):
        ql = z_ref[:, Z_QL:Z_QL + QL].astype(F32)
        kvl = z_ref[:, Z_KV:Z_KV + KVL].astype(F32)
        qn = (ql * _rsq(ql) * qa_ref[...]).astype(MX)
        kvn = (kvl * _rsq(kvl) * kva_ref[...]).astype(MX)
        qn_ref[...] = qn
        kvn_ref[...] = kvn
        qp = jnp.dot(qn, wq_ref[...], preferred_element_type=F32)
        kvp = jnp.dot(kvn, wkv_ref[...], preferred_element_type=F32)
        qp_ref[...] = qp
        kvp_ref[...] = kvp
        m_n, m_r, first_half = _lane_masks((tm, LANE))
        c = cos_ref[...]
        s = sin_ref[...]
        qg = qg_ref[...]
        kg = kg_ref[...]
        xr = z_ref[:, Z_KR:Z_KR + LANE].astype(F32)
        rr = lax.rsqrt(jnp.sum(xr * xr, axis=-1, keepdims=True) * (1.0 / ROPE) + EPS)
        yr = xr * rr * kg
        kr = jnp.where(m_r, yr * c + _rot(yr, first_half) * s, 0.0)
        for h in range(NH):
            x = qp[:, h * LANE:(h + 1) * LANE]
            x2 = x * x
            rn = lax.rsqrt(jnp.sum(jnp.where(m_n, x2, 0.0), axis=-1, keepdims=True) * (1.0 / NOPE) + EPS)
            rq = lax.rsqrt(jnp.sum(jnp.where(m_r, x2, 0.0), axis=-1, keepdims=True) * (1.0 / ROPE) + EPS)
            y = x * jnp.where(m_n, rn, jnp.where(m_r, rq, 0.0)) * qg
            q_ref[:, h * LANE:(h + 1) * LANE] = ((y * c + _rot(y, first_half) * s) * ATTN_SCALE).astype(q_ref.dtype)
            xk = kvp[:, h * LANE:(h + 1) * LANE]
            rk = lax.rsqrt(jnp.sum(jnp.where(m_n, xk * xk, 0.0), axis=-1, keepdims=True) * (1.0 / NOPE) + EPS)
            k_ref[:, h * LANE:(h + 1) * LANE] = (jnp.where(m_n, xk * rk * kg, 0.0) + kr).astype(k_ref.dtype)
        v_ref[...] = kvp[:, NH * LANE:].astype(v_ref.dtype)

    return _pcall(body, name=name, grid=(T // tm,),
                  in_specs=[_row_spec(tm, LAT_W), whole(w_q), whole(w_kv), whole(qa_gain), whole(kva_gain), slab, slab, vec, vec],
                  out_specs=[_row_spec(tm, QL), _row_spec(tm, KVL), _row_spec(tm, NH * LANE), _row_spec(tm, NH * LANE + NH * NOPE),
                             _row_spec(tm, NH * LANE), _row_spec(tm, NH * LANE), _row_spec(tm, NH * NOPE)],
                  out_shape=[_sds((T, QL), MX), _sds((T, KVL), MX), _sds((T, NH * LANE), F32), _sds((T, NH * LANE + NH * NOPE), F32),
                             _sds((T, NH * LANE), MX), _sds((T, NH * LANE), MX), _sds((T, NH * NOPE), MX)],
                  )(z, w_q, w_kv, qa_gain, kva_gain, cos, sin, q_gain, k_gain)


def _mla_in_bwd(dq, dk, dv, qp, kvp, z, w_q, w_kv, qa_gain, kva_gain, cos, sin, q_gain, k_gain, *, nb, name, tm=256):
    T = dq.shape[0]
    tps = (T // nb) // tm
    slab = pl.BlockSpec((tm, LANE), lambda i: (i, 0))
    vec = pl.BlockSpec((1, LANE), lambda i: (0, 0))
    whole = lambda a: pl.BlockSpec(a.shape, lambda i: (0, 0))

    def latent_bwd(x, dy, gain):
        r = _rsq(x)
        xhat = x * r
        dxhat = dy * gain
        return r * (dxhat - xhat * jnp.mean(dxhat * xhat, axis=-1, keepdims=True)), jnp.sum(dy * xhat, axis=0, keepdims=True)

    def body(dq_ref, dk_ref, dv_ref, qp_ref, kvp_ref, z_ref, wq_ref, wkv_ref, qa_ref, kva_ref, cos_ref, sin_ref, qg_ref, kg_ref,
             dqp_ref, dkvp_ref, dkr_ref, dql_ref, dkvl_ref, st_ref, sq_ref, sk_ref):
        i = pl.program_id(0)
        qp, kvp = qp_ref, kvp_ref
        m_n, m_r, first_half = _lane_masks((tm, LANE))
        c = cos_ref[...]
        s = sin_ref[...]
        qg = qg_ref[...]
        kg = kg_ref[...]
        dqg = jnp.zeros((1, LANE), F32)
        dkg = jnp.zeros((1, LANE), F32)
        dkr_sum = jnp.zeros((tm, LANE), F32)
        for h in range(NH):
            x = qp[:, h * LANE:(h + 1) * LANE]
            x2 = x * x
            rn = lax.rsqrt(jnp.sum(jnp.where(m_n, x2, 0.0), axis=-1, keepdims=True) * (1.0 / NOPE) + EPS)
            rq = lax.rsqrt(jnp.sum(jnp.where(m_r, x2, 0.0), axis=-1, keepdims=True) * (1.0 / ROPE) + EPS)
            rfac = jnp.where(m_n, rn, jnp.where(m_r, rq, 0.0))
            xhat = x * rfac
            do = dq_ref[:, h * LANE:(h + 1) * LANE] * ATTN_SCALE
            dy = do * c + _rot_t(do * s, first_half, m_r)
            dqg = dqg + jnp.sum(dy * xhat, axis=0, keepdims=True)
            dxhat = dy * qg
            t = dxhat * xhat
            mean_n = jnp.sum(jnp.where(m_n, t, 0.0), axis=-1, keepdims=True) * (1.0 / NOPE)
            mean_r = jnp.sum(jnp.where(m_r, t, 0.0), axis=-1, keepdims=True) * (1.0 / ROPE)
            dqp_ref[:, h * LANE:(h + 1) * LANE] = (
                rfac * (dxhat - xhat * jnp.where(m_n, mean_n, jnp.where(m_r, mean_r, 0.0)))).astype(dqp_ref.dtype)

            xk = kvp[:, h * LANE:(h + 1) * LANE]
            rk = lax.rsqrt(jnp.sum(jnp.where(m_n, xk * xk, 0.0), axis=-1, keepdims=True) * (1.0 / NOPE) + EPS)
            khat = jnp.where(m_n, xk * rk, 0.0)
            dko = dk_ref[:, h * LANE:(h + 1) * LANE]
            dkn = jnp.where(m_n, dko, 0.0)
            dkg = dkg + jnp.sum(dkn * khat, axis=0, keepdims=True)
            dkhat = dkn * kg
            mean_k = jnp.sum(dkhat * khat, axis=-1, keepdims=True) * (1.0 / NOPE)
            dkvp_ref[:, h * LANE:(h + 1) * LANE] = jnp.where(m_n, rk * (dkhat - khat * mean_k), 0.0).astype(dkvp_ref.dtype)
            dkr_sum = dkr_sum + jnp.where(m_r, dko, 0.0)
        dkvp_ref[:, NH * LANE:] = dv_ref[...].astype(dkvp_ref.dtype)

        xr = z_ref[:, Z_KR:Z_KR + LANE].astype(F32)
        rr = lax.rsqrt(jnp.sum(xr * xr, axis=-1, keepdims=True) * (1.0 / ROPE) + EPS)
        rhat = xr * rr
        dyr = dkr_sum * c + _rot_t(dkr_sum * s, first_half, m_r)
        dkg = dkg + jnp.sum(dyr * rhat, axis=0, keepdims=True)
        drhat = dyr * kg
        mean_kr = jnp.sum(drhat * rhat, axis=-1, keepdims=True) * (1.0 / ROPE)
        dkr_ref[...] = jnp.where(m_r, rr * (drhat - rhat * mean_kr), 0.0).astype(dkr_ref.dtype)

        nt = (((1,), (1,)), ((), ()))
        dqn = lax.dot_general(dqp_ref[...], wq_ref[...], nt, preferred_element_type=F32)
        dkvn = lax.dot_general(dkvp_ref[...], wkv_ref[...], nt, preferred_element_type=F32)
        dql, dqa = latent_bwd(z_ref[:, Z_QL:Z_QL + QL].astype(F32), dqn, qa_ref[...])
        dkvl, dkva = latent_bwd(z_ref[:, Z_KV:Z_KV + KVL].astype(F32), dkvn, kva_ref[...])
        dql_ref[...] = dql.astype(dql_ref.dtype)
        dkvl_ref[...] = dkvl.astype(dkvl_ref.dtype)

        @pl.when(i % tps == 0)
        def _():
            st_ref[...] = jnp.zeros_like(st_ref)
            sq_ref[...] = jnp.zeros_like(sq_ref)
            sk_ref[...] = jnp.zeros_like(sk_ref)

        st_ref[0, 0:1, :] += dqg
        st_ref[0, 1:2, :] += dkg
        sq_ref[0, 0:1, :] += dqa
        sk_ref[0, 0:1, :] += dkva

    stat = lambda w: pl.BlockSpec((1, SUBLANE, w), lambda i: (i // tps, 0, 0))
    return _pcall(body, name=name, grid=(T // tm,),
                  in_specs=[_row_spec(tm, NH * LANE), _row_spec(tm, NH * LANE), _row_spec(tm, NH * NOPE),
                            _row_spec(tm, NH * LANE), _row_spec(tm, NH * LANE + NH * NOPE), _row_spec(tm, LAT_W),
                            whole(w_q), whole(w_kv), whole(qa_gain), whole(kva_gain), slab, slab, vec, vec],
                  out_specs=[_row_spec(tm, NH * LANE), _row_spec(tm, NH * LANE + NH * NOPE), slab, _row_spec(tm, QL),
                             _row_spec(tm, KVL), stat(LANE), stat(QL), stat(KVL)],
                  out_shape=[_sds((T, NH * LANE), MX), _sds((T, NH * LANE + NH * NOPE), MX), _sds((T, LANE), MX),
                             _sds((T, QL), MX), _sds((T, KVL), MX), _sds((nb, SUBLANE, LANE), F32),
                             _sds((nb, SUBLANE, QL), F32), _sds((nb, SUBLANE, KVL), F32)],
                  )(dq, dk, dv, qp, kvp, z, w_q, w_kv, qa_gain, kva_gain, cos, sin, q_gain, k_gain)


def _lower_triangle(t):
    return lax.broadcasted_iota(jnp.int32, (t, t), 1) <= lax.broadcasted_iota(jnp.int32, (t, t), 0)


def _attn_fwd(q, k, v, *, nb, name, tq=512, comm=None):
    T = q.shape[0]
    S = T // nb
    tq = _pick(S, tq)
    nq = S // tq
    tk = tq
    npair = NH // 2

    def body(q_ref, k_ref, v_ref, o_ref, lse_ref):
        qi = pl.program_id(2)
        lane = lax.broadcasted_iota(jnp.int32, (tq, LANE), 1)
        qs = [q_ref[:, hh * LANE:(hh + 1) * LANE] for hh in range(2)]

        def block(j, carry, diagonal):
            k0 = pl.multiple_of(j * tk, tk)
            vb = v_ref[pl.ds(k0, tk), :]
            new = []
            for hh in range(2):
                m, l, acc = carry[hh]
                kb = k_ref[pl.ds(k0, tk), hh * LANE:(hh + 1) * LANE]
                s = lax.dot_general(qs[hh], kb, (((1,), (1,)), ((), ())), preferred_element_type=F32)
                if diagonal:
                    s = jnp.where(_lower_triangle(tq), s, NEG)
                m_new = jnp.maximum(m, jnp.max(s, axis=-1, keepdims=True))
                p = jnp.exp(s - m_new)
                alpha = jnp.exp(m - m_new)
                l = alpha * l + jnp.sum(p, axis=-1, keepdims=True)
                acc = alpha * acc + jnp.dot(p.astype(MX), vb, preferred_element_type=F32)
                new.append((m_new, l, acc))
            return tuple(new)

        init = tuple((jnp.full((tq, 1), NEG, F32), jnp.zeros((tq, 1), F32), jnp.zeros((tq, LANE), F32)) for _ in range(2))
        carry = lax.fori_loop(0, qi, lambda j, c: block(j, c, False), init)
        (m0, l0, acc0), (m1, l1, acc1) = block(qi, carry, True)
        o_ref[...] = jnp.where(lane < NOPE, acc0 / l0, acc1 / l1).astype(o_ref.dtype)
        lse_ref[...] = jnp.where(lane < NOPE, m0 + jnp.log(l0), m1 + jnp.log(l1))

    return _pcall(body, name=name, grid=(nb, npair, nq),
                  in_specs=[pl.BlockSpec((tq, 2 * LANE), lambda b, p, i: (b * nq + i, p)),
                            pl.BlockSpec((S, 2 * LANE), lambda b, p, i: (b, p)),
                            pl.BlockSpec((S, LANE), lambda b, p, i: (b, p))],
                  out_specs=[pl.BlockSpec((tq, LANE), lambda b, p, i: (b * nq + i, p)),
                             pl.BlockSpec((tq, LANE), lambda b, p, i: (b * nq + i, p))],
                  out_shape=[_sds((T, NH * NOPE), MX), _sds((T, NH * NOPE), F32)], comm=comm)(q, k, v)


def _attn_bwd(q, k, v, o, lse, do, *, nb, name, tq=512, comm=None):
    T = q.shape[0]
    S = T // nb
    tq = _pick(S, tq)
    nq = S // tq
    tk = tq
    npair = NH // 2

    def body(q_ref, k_ref, v_ref, o_ref, lse_ref, do_ref, dq_ref, dk_ref, dv_ref, delta_ref):
        lane = lax.broadcasted_iota(jnp.int32, (tq, LANE), 1)
        first = lane < NOPE
        dq_ref[...] = jnp.zeros_like(dq_ref)

        def delta_step(qi, _):
            q0 = pl.multiple_of(qi * tq, tq)
            prod = do_ref[pl.ds(q0, tq), :] * o_ref[pl.ds(q0, tq), :].astype(F32)
            d0 = jnp.sum(jnp.where(first, prod, 0.0), axis=-1, keepdims=True)
            d1 = jnp.sum(jnp.where(first, 0.0, prod), axis=-1, keepdims=True)
            delta_ref[pl.ds(q0, tq), :] = jnp.where(first, d0, d1)
            return 0

        lax.fori_loop(0, nq, delta_step, 0)

        def kv_step(kj, _):
            k0 = pl.multiple_of(kj * tk, tk)
            kbs = [k_ref[pl.ds(k0, tk), hh * LANE:(hh + 1) * LANE] for hh in range(2)]
            vb = v_ref[pl.ds(k0, tk), :]

            def q_block(qi, carry, diagonal):
                dk0, dk1, dv = carry
                dks = [dk0, dk1]
                q0 = pl.multiple_of(qi * tq, tq)
                dov = do_ref[pl.ds(q0, tq), :]
                lse_v = lse_ref[pl.ds(q0, tq), :]
                delta_v = delta_ref[pl.ds(q0, tq), :]
                for hh in range(2):
                    qh = q_ref[pl.ds(q0, tq), hh * LANE:(hh + 1) * LANE]
                    dob = jnp.where(first if hh == 0 else jnp.logical_not(first), dov, 0.0).astype(MX)
                    s = lax.dot_general(qh, kbs[hh], (((1,), (1,)), ((), ())), preferred_element_type=F32)
                    p = jnp.exp(s - lse_v[:, hh * NOPE:hh * NOPE + 1])
                    if diagonal:
                        p = jnp.where(_lower_triangle(tq), p, 0.0)
                    dp = lax.dot_general(dob, vb, (((1,), (1,)), ((), ())), preferred_element_type=F32)
                    ds = (p * (dp - delta_v[:, hh * NOPE:hh * NOPE + 1])).astype(MX)
                    dks[hh] = dks[hh] + lax.dot_general(ds, qh, (((0,), (0,)), ((), ())), preferred_element_type=F32)
                    dv = dv + lax.dot_general(p.astype(MX), dob, (((0,), (0,)), ((), ())), preferred_element_type=F32)
                    dq_ref[pl.ds(q0, tq), hh * LANE:(hh + 1) * LANE] += jnp.dot(ds, kbs[hh], preferred_element_type=F32)
                return dks[0], dks[1], dv

            zero = jnp.zeros((tk, LANE), F32)
            carry = q_block(kj, (zero, zero, zero), True)
            dk0, dk1, dv = lax.fori_loop(kj + 1, nq, lambda qi, c: q_block(qi, c, False), carry)
            dk_ref[pl.ds(k0, tk), 0:LANE] = dk0
            dk_ref[pl.ds(k0, tk), LANE:2 * LANE] = dk1
            dv_ref[pl.ds(k0, tk), :] = dv
            return 0

        lax.fori_loop(0, nq, kv_step, 0)

    pair256 = pl.BlockSpec((S, 2 * LANE), lambda b, p: (b, p))
    pair128 = pl.BlockSpec((S, LANE), lambda b, p: (b, p))
    return _pcall(body, name=name, grid=(nb, npair),
                  in_specs=[pair256, pair256, pair128, pair128, pair128, pair128],
                  out_specs=[pair256, pair256, pair128],
                  out_shape=[_sds((T, NH * LANE), F32), _sds((T, NH * LANE), F32), _sds((T, NH * NOPE), F32)],
                  scratch=[pltpu.VMEM((S, LANE), F32)], comm=comm)(q, k, v, o, lse, do)


def _run(plan, fn, *args, name, **kw):
    rider = plan.rider(name)
    if rider is None:
        return fn(*args, name=name, **kw)
    outs, landed = fn(*args, name=name, comm=rider, **kw)
    plan.landed(name, landed)
    return outs


def _layer_fwd_bwd(x, tgt, mod3, cos, sin, plan, P):
    nb = mod3.shape[0]
    W = plan.W
    h1, gu1, a1 = _run(plan, _ffn_up_first, x, mod3, P["norm_ffn1"], W["ffn1_in"], sub=0, name="ffn1_up_a")
    gu1, a1 = _run(plan, _ffn_up_second, h1, W["ffn1_in"], gu1, a1, name="ffn1_up_b")
    f1, x1, h2 = _run(plan, _mm_resid_norm, a1, W["ffn1_out"], x, mod3, P["norm_mix"], sub=1, coef=0.5, name="ffn1_out")
    z = _run(plan, _mm, h2, W["w_in"], mode="nt", out_dtype=MX, name="mix_in", tn=1664)
    pooled, mixed, scaled = _pool_fwd(z, W["pool_grp"], P["pool_scale"], nb=nb, name="pool_fwd")
    br_pool = _mm(scaled, W["pool_proj"], mode="nn", out_dtype=MX, name="pool_proj")
    qn, kvn, qp, kvp, q, k, v = _mla_in_fwd(z, W["q_up"], W["kv_up"], P["q_a_norm"], P["kv_a_norm"], cos, sin,
                                            P["q_gain"], P["k_gain"], name="mla_in_fwd")
    attn, lse = _run(plan, _attn_fwd, q, k, v, nb=nb, name="attn_fwd")
    br_mla, merged, mo, x2, h3 = _mix_out_fwd(z, br_pool, attn, W["mla_proj"], W["w_out"], x1, mod3, P["norm_ffn2"], sub=2,
                                              name="mix_out", tm=256)
    gu2, a2 = _ffn_up(h3, W["ffn2_in"], name="ffn2_up")
    dy, df2, st_fin, loss = _mm_loss(a2, W["ffn2_out"], x2, tgt, mod3, name="ffn2_out_loss")

    plan.grad("ffn2_out", _dw(a2, df2, name="d_ffn2_out"))
    dgu2 = _ffn_dact(df2, gu2, W["ffn2_out"], name="ffn2_dact")
    plan.grad("ffn2_in", _ffn_dw_in(h3, dgu2, name="d_ffn2_in"))
    dx2, dmo, st3 = _run(plan, _ffn_dh_norm, dgu2, W["ffn2_in"], x2, dy, mo, mod3, P["norm_ffn2"], sub=2, coef=1.0,
                         name="d_ffn2_h")
    plan.grad("w_out", _dw(merged, dmo, name="d_mix_out"))
    dbr_pool, dbr_mla, dgates, dattn = _mix_out_bwd(dmo, W["w_out"], z, br_pool, br_mla, W["mla_proj"], name="mix_out_bwd", tm=256)
    plan.grad("pool_proj", _dw(scaled, dbr_pool, name="d_pool_proj"))
    dscaled = _mm(dbr_pool, W["pool_proj"], mode="nt", out_dtype=F32, name="d_pool_scaled")
    du_pool, d_pool_grp, d_pool_scale = _pool_bwd(dscaled, mixed, pooled, W["pool_grp"], P["pool_scale"], nb=nb, name="pool_bwd")
    plan.grad("mla_proj", _dw(attn, dbr_mla, name="d_mla_proj"))
    dq, dk, dv = _run(plan, _attn_bwd, q, k, v, attn, lse, dattn, nb=nb, name="attn_bwd")
    dqp, dkvp, dkr, dql, dkvl, st_prep, st_q, st_kv = _mla_in_bwd(
        dq, dk, dv, qp, kvp, z, W["q_up"], W["kv_up"], P["q_a_norm"], P["kv_a_norm"], cos, sin, P["q_gain"], P["k_gain"],
        nb=nb, name="mla_in_bwd")
    plan.grad("q_up", _dw(qn, dqp, name="d_q_up"))
    plan.grad("kv_up", _dw(kvn, dkvp, name="d_kv_up"))
    dz = jnp.concatenate([du_pool, dql, dkvl, dkr, dgates], axis=1)
    plan.grad("w_in", _run(plan, _dw, h2, dz, name="d_mix_in", tm=256, tn=1664, out_t=True))
    dx1, df1, st2 = _run(plan, _mm_norm_bwd, dz, W["w_in"], x1, dx2, f1, mod3, P["norm_mix"], sub=1, coef=0.5, name="d_mix_h")
    plan.grad("ffn1_out", _run(plan, _dw, a1, df1, name="d_ffn1_out"))
    dgu1 = _run(plan, _ffn_dact, df1, gu1, W["ffn1_out"], name="ffn1_dact")
    half = x.shape[1] // 2
    plan.grad("ffn1_in@0", _run(plan, _ffn_dw_in, h1, dgu1, rows=(0, half), name="d_ffn1_in_a"))
    plan.grad("ffn1_in@1", _run(plan, _ffn_dw_in, h1, dgu1, rows=(half, half), name="d_ffn1_in_b"))
    dh1 = _run(plan, _ffn_dh, dgu1, W["ffn1_in"], name="d_ffn1_h")
    grad_x, st1 = _run(plan, _bwd_block, x, dh1, dx1, None, mod3, P["norm_ffn1"], sub=0, coef=0.0, name="bwd_norm1")

    return loss, grad_x, (st1, st2, st3, st_fin, st_q, st_kv, st_prep, d_pool_scale), d_pool_grp


def _w_in_to_kernel(wt):
    zeros = lambda n: jnp.zeros((n, wt.shape[1]), wt.dtype)
    return jnp.concatenate([wt[0:1152], zeros(KR_LANE), wt[1152:1184], zeros(LANE - KR_LANE - ROPE), wt[1184:]], axis=0)


def _w_in_from_kernel(gt):
    return jnp.concatenate([gt[0:1152], gt[Z_KR + KR_LANE:Z_KR + KR_LANE + ROPE], gt[Z_GP:]], axis=0)


def _pad_rows(a):
    rows = a.shape[-2]
    pad = [(0, 0)] * a.ndim
    pad[-2] = (0, -rows % (4 * SUBLANE))
    return jnp.pad(a, pad)


def _q_up_to_kernel(w):
    k = w.shape[0]
    return jnp.pad(w.reshape(k, NH, NOPE + ROPE), ((0, 0), (0, 0), (0, LANE - NOPE - ROPE))).reshape(k, NH * LANE)


def _q_up_from_kernel(g):
    k = g.shape[0]
    return g.reshape(k, NH, LANE)[:, :, :NOPE + ROPE].reshape(k, NH * (NOPE + ROPE))


def _kv_up_to_kernel(w):
    k = w.shape[0]
    w3 = w.reshape(k, NH, 2 * NOPE)
    kpart = jnp.pad(w3[:, :, :NOPE], ((0, 0), (0, 0), (0, LANE - NOPE))).reshape(k, NH * LANE)
    return jnp.concatenate([kpart, w3[:, :, NOPE:].reshape(k, NH * NOPE)], axis=1)


def _kv_up_from_kernel(g):
    k = g.shape[0]
    kpart = g[:, :NH * LANE].reshape(k, NH, LANE)[:, :, :NOPE]
    vpart = g[:, NH * LANE:].reshape(k, NH, NOPE)
    return jnp.concatenate([kpart, vpart], axis=2).reshape(k, NH * 2 * NOPE)


def _gain_slab(nope, rope):
    return jnp.concatenate([nope, rope, jnp.zeros((1, LANE - NOPE - ROPE), nope.dtype)], axis=1)


def _rope_tables(positions):
    inv_freq = 10000.0 ** (-jnp.arange(0, ROPE, 2, dtype=F32) / ROPE)
    ang = positions.astype(F32)[:, None] * inv_freq
    ang = jnp.concatenate([ang, ang], axis=-1)
    t = positions.shape[0]
    cos = jnp.concatenate([jnp.ones((t, KR_LANE), F32), jnp.cos(ang), jnp.ones((t, LANE - KR_LANE - ROPE), F32)], axis=1)
    sin = jnp.concatenate([jnp.zeros((t, KR_LANE), F32), jnp.sin(ang), jnp.zeros((t, LANE - KR_LANE - ROPE), F32)], axis=1)
    return cos, sin


def _coords():
    return lax.axis_index("x"), lax.axis_index("y"), lax.axis_index("c")


HBM_SPEC = pl.BlockSpec(memory_space=pl.ANY)
VMEM_SPEC = pl.BlockSpec(memory_space=pltpu.VMEM)


CHIP_RELS = ((1, 0), (0, 1), (1, 1))


def _gather_comm(shards):
    n = len(shards)

    def ici(ins, outs, sems, a, j, x, y, cc):
        half = ins[a].shape[0] // 2
        mine = pl.ds(cc * half, half)
        dx, dy = CHIP_RELS[j]
        return pltpu.make_async_remote_copy(src_ref=ins[a].at[mine], dst_ref=outs[a].at[2 * x + y, mine],
                                            send_sem=sems[0].at[a, j], recv_sem=sems[1].at[a, j],
                                            device_id=(x ^ dx, y ^ dy, cc), device_id_type=MESH)

    def d2d(ins, outs, sems, a, j, x, y, cc, half_of):
        half = ins[a].shape[0] // 2
        dx, dy = CHIP_RELS[j]
        landed = outs[a].at[2 * (x ^ dx) + (y ^ dy), pl.ds(half_of * half, half)]
        return pltpu.make_async_remote_copy(src_ref=landed, dst_ref=landed, send_sem=sems[2].at[a, j], recv_sem=sems[3].at[a, j],
                                            device_id=(x, y, 1 - cc), device_id_type=MESH)

    def own(ins, outs, sems, a, x, y):
        return pltpu.make_async_copy(ins[a], outs[a].at[2 * x + y], sems[4].at[a])

    def start(ins, outs, sems):
        x, y, cc = _coords()
        for a in range(n):
            own(ins, outs, sems, a, x, y).start()
            for j in range(3):
                ici(ins, outs, sems, a, j, x, y, cc).start()

    def finish(ins, outs, sems):
        x, y, cc = _coords()
        for a in range(n):
            for j in range(3):
                ici(ins, outs, sems, a, j, x, y, cc).wait_recv()
                d2d(ins, outs, sems, a, j, x, y, cc, cc).start()
        for a in range(n):
            for j in range(3):
                d2d(ins, outs, sems, a, j, x, y, cc, 1 - cc).wait_recv()
        for a in range(n):
            for j in range(3):
                ici(ins, outs, sems, a, j, x, y, cc).wait_send()
                d2d(ins, outs, sems, a, j, x, y, cc, cc).wait_send()
            own(ins, outs, sems, a, x, y).wait()

    dma = pltpu.SemaphoreType.DMA
    return _Comm(shards, [_sds((N_CHIP,) + s.shape, s.dtype) for s in shards],
                 [dma((n, 3)), dma((n, 3)), dma((n, 3)), dma((n, 3)), dma((n,))], start, finish)


def _swap_comm(parts):
    n = len(parts)

    def copy(ins, outs, sems, a):
        x, y, cc = _coords()
        half = ins[a].shape[1] // 2
        return pltpu.make_async_remote_copy(src_ref=ins[a].at[:, pl.ds((1 - cc) * half, half)], dst_ref=outs[a],
                                            send_sem=sems[0].at[a], recv_sem=sems[1].at[a], device_id=(x, y, 1 - cc),
                                            device_id_type=MESH)

    def start(ins, outs, sems):
        for a in range(n):
            copy(ins, outs, sems, a).start()

    def finish(ins, outs, sems):
        for a in range(n):
            copy(ins, outs, sems, a).wait()

    dma = pltpu.SemaphoreType.DMA
    return _Comm(parts, [_sds((p.shape[0], p.shape[1] // 2, p.shape[2]), p.dtype) for p in parts], [dma((n,)), dma((n,))],
                 start, finish)


def _add_half(full, other, cidx, *, name):
    nch, r, c = full.shape
    half = r // 2
    tr = _pick_rows(half)
    nbk = half // tr
    grid_spec = pltpu.PrefetchScalarGridSpec(
        num_scalar_prefetch=1, grid=(nch, nbk),
        in_specs=[pl.BlockSpec((1, tr, c), lambda j, i, cref: (j, cref[0] * nbk + i, 0)),
                  pl.BlockSpec((1, tr, c), lambda j, i, cref: (j, i, 0))],
        out_specs=pl.BlockSpec((1, tr, c), lambda j, i, cref: (j, i, 0)))

    def body(cref, a_ref, b_ref, o_ref):
        o_ref[...] = (a_ref[...] + b_ref[...]).astype(o_ref.dtype)

    return _pcall(body, name=name, out_shape=_sds((nch, half, c), MX), grid_spec=grid_spec)(cidx, full, other)


def _pick_rows(rows, target=512):
    best = None
    for t in range(16, min(rows, target) + 1, 16):
        if rows % t == 0:
            best = t
    return rows if best is None else best


def _exchange_comm(parts):
    n = len(parts)

    def send(ins, outs, sems, a, j, x, y, cc):
        dx, dy = CHIP_RELS[j]
        return pltpu.make_async_remote_copy(src_ref=ins[a].at[2 * (x ^ dx) + (y ^ dy)], dst_ref=outs[a].at[2 * x + y],
                                            send_sem=sems[0].at[a, j], recv_sem=sems[1].at[a, j],
                                            device_id=(x ^ dx, y ^ dy, cc), device_id_type=MESH)

    def landing(ins, outs, sems, a, j, x, y, cc):
        dx, dy = CHIP_RELS[j]
        peer_chip = 2 * (x ^ dx) + (y ^ dy)
        return pltpu.make_async_remote_copy(src_ref=ins[a].at[peer_chip], dst_ref=outs[a].at[peer_chip], send_sem=sems[0].at[a, j],
                                            recv_sem=sems[1].at[a, j], device_id=(x, y, cc), device_id_type=MESH)

    def own(ins, outs, sems, a, x, y):
        return pltpu.make_async_copy(ins[a].at[2 * x + y], outs[a].at[2 * x + y], sems[2].at[a])

    def start(ins, outs, sems):
        x, y, cc = _coords()
        for a in range(n):
            own(ins, outs, sems, a, x, y).start()
            for j in range(3):
                send(ins, outs, sems, a, j, x, y, cc).start()

    def finish(ins, outs, sems):
        x, y, cc = _coords()
        for a in range(n):
            for j in range(3):
                landing(ins, outs, sems, a, j, x, y, cc).wait_recv()
        for a in range(n):
            for j in range(3):
                send(ins, outs, sems, a, j, x, y, cc).wait_send()
            own(ins, outs, sems, a, x, y).wait()

    dma = pltpu.SemaphoreType.DMA
    return _Comm(parts, [_sds(p.shape, p.dtype) for p in parts], [dma((n, 3)), dma((n, 3)), dma((n,))], start, finish)


def _sum_chips(q, cidx, *, name):
    nch, h, c = q.shape
    tr = _pick_rows(h)
    nbk = h // tr
    grid_spec = pltpu.PrefetchScalarGridSpec(
        num_scalar_prefetch=1, grid=(nbk,),
        in_specs=[pl.BlockSpec((nch, tr, c), lambda i, cref: (0, i, 0))],
        out_specs=pl.BlockSpec((tr, c), lambda i, cref: (cref[0] * nbk + i, 0)))

    def body(cref, q_ref, o_ref):
        acc = q_ref[0].astype(F32) + q_ref[1].astype(F32)
        acc = acc + q_ref[2].astype(F32)
        o_ref[...] = acc + q_ref[3].astype(F32)

    return _pcall(body, name=name, out_shape=_sds((2 * h, c), F32), grid_spec=grid_spec)(cidx, q)


def _join_comm(fulls):
    n = len(fulls)

    def half_copy(outs, sems, a, which):
        x, y, cc = _coords()
        h = outs[a].shape[0] // 2
        rows = outs[a].at[pl.ds((cc if which == 0 else 1 - cc) * h, h)]
        return pltpu.make_async_remote_copy(src_ref=rows, dst_ref=rows, send_sem=sems[0].at[a], recv_sem=sems[1].at[a],
                                            device_id=(x, y, 1 - cc), device_id_type=MESH)

    def start(ins, outs, sems):
        for a in range(n):
            half_copy(outs, sems, a, 0).start()

    def finish(ins, outs, sems):
        for a in range(n):
            half_copy(outs, sems, a, 1).wait_recv()
        for a in range(n):
            half_copy(outs, sems, a, 0).wait_send()

    dma = pltpu.SemaphoreType.DMA
    return _Comm(fulls, [_sds(p.shape, p.dtype) for p in fulls], [dma((n,)), dma((n,))], start, finish,
                 aliases={a: a for a in range(n)})


def _ada_prologue(c, w, b, *, name, comm=None):
    nb, dm = c.shape
    n = w.shape[1]

    def body(c_ref, w_ref, b_ref, call_ref, land_ref, cpad, mod_s, g_send, g_recv, m_send, m_recv):
        x, y, cc = _coords()
        me = 4 * x + 2 * y + cc
        chip = 2 * x + y
        cpad[...] = jnp.zeros_like(cpad)
        cpad[0:nb, :] = c_ref[...]
        copies = []
        for kk in range(1, N_DEV):
            peer = (x ^ (kk >> 2), y ^ ((kk >> 1) & 1), cc ^ (kk & 1))
            cp = pltpu.make_async_remote_copy(src_ref=cpad, dst_ref=call_ref.at[me], send_sem=g_send.at[kk - 1],
                                              recv_sem=g_recv.at[kk - 1], device_id=peer, device_id_type=MESH)
            cp.start()
            copies.append(cp)
        call_ref[me] = cpad[...]
        for kk in range(1, N_DEV):
            pltpu.make_async_remote_copy(src_ref=cpad, dst_ref=call_ref.at[me ^ kk], send_sem=g_send.at[kk - 1],
                                         recv_sem=g_recv.at[kk - 1], device_id=(x, y, cc), device_id_type=MESH).wait_recv()
        cv = call_ref[...].reshape(N_DEV * SUBLANE, dm)
        act = (cv * _sigmoid(cv)).astype(MX)
        mod = jnp.dot(act, w_ref[...].astype(MX), preferred_element_type=F32) + b_ref[...]
        mod_s[...] = mod.reshape(N_DEV, SUBLANE, n)
        for j, (dx, dy) in enumerate(CHIP_RELS):
            cp = pltpu.make_async_remote_copy(src_ref=mod_s.at[4 * (x ^ dx) + 2 * (y ^ dy) + cc], dst_ref=land_ref.at[chip],
                                              send_sem=m_send.at[j], recv_sem=m_recv.at[j],
                                              device_id=(x ^ dx, y ^ dy, cc), device_id_type=MESH)
            cp.start()
            copies.append(cp)
        land_ref[chip] = mod_s[me]
        for j, (dx, dy) in enumerate(CHIP_RELS):
            pltpu.make_async_remote_copy(src_ref=mod_s.at[me], dst_ref=land_ref.at[2 * (x ^ dx) + (y ^ dy)], send_sem=m_send.at[j],
                                         recv_sem=m_recv.at[j], device_id=(x, y, cc), device_id_type=MESH).wait_recv()
        for cp in copies:
            cp.wait_send()

    dma = pltpu.SemaphoreType.DMA
    return _pcall(body, name=name, in_specs=[VMEM_SPEC] * 3, out_specs=[VMEM_SPEC] * 2,
                  out_shape=[_sds((N_DEV, SUBLANE, dm), F32), _sds((N_CHIP, SUBLANE, n), F32)],
                  scratch=[pltpu.VMEM((SUBLANE, dm), F32), pltpu.VMEM((N_DEV, SUBLANE, n), F32),
                           dma((N_DEV - 1,)), dma((N_DEV - 1,)), dma((3,)), dma((3,))], comm=comm)(c, w, b)


def _ada_bwd(c_all, dmod_cols, *, name):
    m, kdim = c_all.shape
    n = dmod_cols.shape[1]
    tn = _pick(n, 1152)

    def body(c_ref, d_ref, o_ref):
        cv = c_ref[...]
        act = (cv * _sigmoid(cv)).astype(MX)
        o_ref[...] = lax.dot_general(act, d_ref[...].astype(MX), (((0,), (0,)), ((), ())), preferred_element_type=F32)

    return _pcall(body, name=name, out_shape=_sds((kdim, n), F32), grid=(n // tn,),
                  in_specs=[pl.BlockSpec((m, kdim), lambda j: (0, 0)), pl.BlockSpec((m, tn), lambda j: (0, j))],
                  out_specs=pl.BlockSpec((kdim, tn), lambda j: (0, j)))(c_all, dmod_cols)


SLAB_W = 1024
RED_ROWS = 8


LOSS_LANE = SLAB_W - LANE


def _pack_stats(st1, st2, st3, st_fin, st_q, st_kv, st_prep, d_pool_scale, loss8, *, name):
    nb = st1.shape[0]
    dm_rows = -(-9 * nb // SUBLANE) * SUBLANE

    def body(s1, s2, s3, sf, sq, skv, sp, sps, loss_ref, o_ref):
        o_ref[...] = jnp.zeros_like(o_ref)
        for s in range(nb):
            rows = [s1[s, 0:1, :], s1[s, 1:2, :], s2[s, 3:4, :], s2[s, 0:1, :], s2[s, 1:2, :], s3[s, 3:4, :], s3[s, 0:1, :],
                    s3[s, 1:2, :], sf[s, 0:1, :]]
            for k, row in enumerate(rows):
                o_ref[9 * s + k:9 * s + k + 1, :] = row

        def over_seq(ref, r):
            acc = ref[0, r:r + 1, :]
            for s in range(1, nb):
                acc = acc + ref[s, r:r + 1, :]
            return acc

        o_ref[dm_rows + 0:dm_rows + 1, :] = over_seq(s1, 2)
        o_ref[dm_rows + 1:dm_rows + 2, :] = over_seq(s2, 2)
        o_ref[dm_rows + 2:dm_rows + 3, :] = over_seq(s3, 2)
        o_ref[dm_rows + 3:dm_rows + 4, 0:POOL_W] = over_seq(sps, 0)
        o_ref[dm_rows + 4:dm_rows + 5, 0:QL] = over_seq(sq, 0)
        o_ref[dm_rows + 5:dm_rows + 6, 0:KVL] = over_seq(skv, 0)
        o_ref[dm_rows + 6:dm_rows + 7, 0:LANE] = over_seq(sp, 0)
        o_ref[dm_rows + 7:dm_rows + 8, 0:LANE] = over_seq(sp, 1)
        o_ref[dm_rows + 7:dm_rows + 8, LOSS_LANE:] = loss_ref[0:1, :]

    return _pcall(body, name=name, out_shape=_sds((dm_rows + RED_ROWS, SLAB_W), F32), in_specs=[VMEM_SPEC] * 9,
                  out_specs=VMEM_SPEC)(st1, st2, st3, st_fin, st_q, st_kv, st_prep, d_pool_scale, loss8)


def _small_allreduce(slab, pool, *, name, comm=None):
    rows, w = slab.shape
    dm = rows - RED_ROWS
    prow, pw = pool.shape
    crow = RED_ROWS + 2 * dm

    def body(slab_ref, pool_ref, red_ref, dmod_ref, ptot_ref, sib_slab, sib_pool, chip_slab, chip_pool, land_slab, land_pool,
             a_send, a_recv, b_send, b_recv):
        x, y, cc = _coords()
        chip = 2 * x + y
        sib = (x, y, 1 - cc)
        to_sib = [pltpu.make_async_remote_copy(src_ref=slab_ref, dst_ref=sib_slab, send_sem=a_send.at[0], recv_sem=a_recv.at[0],
                                               device_id=sib, device_id_type=MESH),
                  pltpu.make_async_remote_copy(src_ref=pool_ref, dst_ref=sib_pool, send_sem=a_send.at[1], recv_sem=a_recv.at[1],
                                               device_id=sib, device_id_type=MESH)]
        for cp in to_sib:
            cp.start()
        for cp in to_sib:
            cp.wait()
        mine_dm, theirs_dm = slab_ref[0:dm, :], sib_slab[0:dm, :]
        chip_slab[0:RED_ROWS, :] = slab_ref[dm:, :] + sib_slab[dm:, :]
        chip_slab[RED_ROWS:RED_ROWS + dm, :] = jnp.where(cc == 0, mine_dm, theirs_dm)
        chip_slab[RED_ROWS + dm:, :] = jnp.where(cc == 0, theirs_dm, mine_dm)
        chip_pool[...] = pool_ref[...] + sib_pool[...]

        sends = []
        for j, (dx, dy) in enumerate(CHIP_RELS):
            peer = (x ^ dx, y ^ dy, cc)
            sends.append(pltpu.make_async_remote_copy(src_ref=chip_slab, dst_ref=land_slab.at[chip], send_sem=b_send.at[j, 0],
                                                      recv_sem=b_recv.at[j, 0], device_id=peer, device_id_type=MESH))
            sends.append(pltpu.make_async_remote_copy(src_ref=chip_pool, dst_ref=land_pool.at[chip], send_sem=b_send.at[j, 1],
                                                      recv_sem=b_recv.at[j, 1], device_id=peer, device_id_type=MESH))
        for cp in sends:
            cp.start()
        land_slab[chip] = chip_slab[...]
        land_pool[chip] = chip_pool[...]
        for j, (dx, dy) in enumerate(CHIP_RELS):
            peer_chip = 2 * (x ^ dx) + (y ^ dy)
            pltpu.make_async_remote_copy(src_ref=chip_slab, dst_ref=land_slab.at[peer_chip], send_sem=b_send.at[j, 0],
                                         recv_sem=b_recv.at[j, 0], device_id=(x, y, cc), device_id_type=MESH).wait_recv()
            pltpu.make_async_remote_copy(src_ref=chip_pool, dst_ref=land_pool.at[peer_chip], send_sem=b_send.at[j, 1],
                                         recv_sem=b_recv.at[j, 1], device_id=(x, y, cc), device_id_type=MESH).wait_recv()
        red = land_slab[0, 0:RED_ROWS, :]
        ptot = land_pool[0]
        for ch in range(1, N_CHIP):
            red = red + land_slab[ch, 0:RED_ROWS, :]
            ptot = ptot + land_pool[ch]
        red_ref[...] = red
        ptot_ref[...] = ptot
        for ch in range(N_CHIP):
            dmod_ref[2 * dm * ch:2 * dm * (ch + 1), :] = land_slab[ch, RED_ROWS:, :]
        for cp in sends:
            cp.wait_send()

    dma = pltpu.SemaphoreType.DMA
    return _pcall(body, name=name, in_specs=[VMEM_SPEC, VMEM_SPEC], out_specs=[VMEM_SPEC] * 3,
                  out_shape=[_sds((RED_ROWS, w), F32), _sds((N_DEV * dm, w), F32), _sds((prow, pw), F32)],
                  scratch=[pltpu.VMEM((rows, w), F32), pltpu.VMEM((prow, pw), F32), pltpu.VMEM((crow, w), F32),
                           pltpu.VMEM((prow, pw), F32), pltpu.VMEM((N_CHIP, crow, w), F32), pltpu.VMEM((N_CHIP, prow, pw), F32),
                           dma((2,)), dma((2,)), dma((3, 2)), dma((3, 2))], comm=comm)(slab, pool)


def _adamw_math(w, g, m, v):
    mn = ADAM_B1 * m + (1.0 - ADAM_B1) * g
    vn = ADAM_B2 * v + (1.0 - ADAM_B2) * (g * g)
    bc1 = 1.0 / (1.0 - ADAM_B1 ** ADAM_STEP)
    bc2 = 1.0 / (1.0 - ADAM_B2 ** ADAM_STEP)
    return -ADAM_LR * ((mn * bc1) / (jnp.sqrt(vn * bc2) + ADAM_EPS) + ADAM_WD * w), mn, vn


def _small_update(red, dmod_all, pool_total, nb, params, *, name):
    names = list(SMALL)
    dm = dmod_all.shape[0] // N_DEV

    def grad_of(nm, red_ref, dmod_ref, ptot_ref):
        if nm == "b_ada":
            acc = None
            for d in range(N_DEV):
                for s in range(nb):
                    blk = dmod_ref[d * dm + 9 * s:d * dm + 9 * s + 9, :]
                    acc = blk if acc is None else acc + blk
            return jnp.concatenate([acc[k:k + 1, :] for k in range(9)], axis=1)
        if nm == "pool_grp":
            return ptot_ref[...]
        row, lo, n = {"norm_ffn1": (0, 0, D), "norm_mix": (1, 0, D), "norm_ffn2": (2, 0, D), "pool_scale": (3, 0, POOL_W),
                      "q_a_norm": (4, 0, QL), "kv_a_norm": (5, 0, KVL), "q_norm_nope": (6, 0, NOPE),
                      "q_norm_rope": (6, NOPE, ROPE), "k_norm_nope": (7, 0, NOPE), "k_norm_rope": (7, KR_LANE, ROPE)}[nm]
        return red_ref[row:row + 1, lo:lo + n]

    def body(*refs):
        red_ref, dmod_ref, ptot_ref = refs[:3]
        ins = refs[3:3 + 3 * len(names)]
        outs = refs[3 + 3 * len(names):]
        outs[4 * len(names)][...] = red_ref[RED_ROWS - 1:RED_ROWS, LOSS_LANE:]
        for i, nm in enumerate(names):
            g = grad_of(nm, red_ref, dmod_ref, ptot_ref)
            d, mn, vn = _adamw_math(ins[3 * i][...], g, ins[3 * i + 1][...], ins[3 * i + 2][...])
            outs[4 * i][...] = g
            outs[4 * i + 1][...] = d
            outs[4 * i + 2][...] = mn
            outs[4 * i + 3][...] = vn

    flat_in = [a for nm in names for a in params[nm]]
    out_shape = [_sds(params[nm][0].shape, F32) for nm in names for _ in range(4)] + [_sds((1, LANE), F32)]
    res = _pcall(body, name=name, in_specs=[VMEM_SPEC] * (3 + len(flat_in)), out_specs=[VMEM_SPEC] * len(out_shape),
                 out_shape=out_shape)(red, dmod_all, pool_total, *flat_in)
    return {nm: tuple(res[4 * i:4 * i + 4]) for i, nm in enumerate(names)}, res[-1]


def _adamw(w, g, m, v, *, name, comm=None):
    r, c = w.shape
    tr = _pick_rows(r, 256)
    tc = c if tr < r else _pick(c, 256)
    spec = pl.BlockSpec((tr, tc), lambda i, j: (i, j))
    bc1 = 1.0 / (1.0 - ADAM_B1 ** ADAM_STEP)
    bc2 = 1.0 / (1.0 - ADAM_B2 ** ADAM_STEP)

    def body(w_ref, g_ref, m_ref, v_ref, d_ref, mo_ref, vo_ref):
        gv = g_ref[...]
        mn = ADAM_B1 * m_ref[...] + (1.0 - ADAM_B1) * gv
        vn = ADAM_B2 * v_ref[...] + (1.0 - ADAM_B2) * (gv * gv)
        mo_ref[...] = mn
        vo_ref[...] = vn
        d_ref[...] = -ADAM_LR * ((mn * bc1) / (jnp.sqrt(vn * bc2) + ADAM_EPS) + ADAM_WD * w_ref[...])

    out = _sds((r, c), F32)
    return _pcall(body, name=name, out_shape=[out, out, out], grid=(r // tr, c // tc), in_specs=[spec] * 4, out_specs=[spec] * 3,
                  comm=comm)(w, g, m, v)


BIG = ("w_ffn1_in", "w_ffn1_out", "w_in", "w_pool_proj", "w_q_up", "w_kv_up", "w_mla_proj", "w_out", "w_ffn2_in", "w_ffn2_out")
ROW_SHARDED = ("w_ffn1_out", "w_out", "w_ffn2_out")
KERNEL_NAME = {"w_ffn1_in": "ffn1_in", "w_ffn1_out": "ffn1_out", "w_in": "w_in", "w_pool_proj": "pool_proj", "w_q_up": "q_up",
               "w_kv_up": "kv_up", "w_mla_proj": "mla_proj", "w_out": "w_out", "w_ffn2_in": "ffn2_in", "w_ffn2_out": "ffn2_out"}
WEIGHTS = ("w_ada", "b_ada", "norm_ffn1", "w_ffn1_in", "w_ffn1_out", "norm_mix", "w_in", "pool_grp", "pool_scale", "w_pool_proj",
           "q_a_norm", "w_q_up", "kv_a_norm", "w_kv_up", "q_norm_nope", "q_norm_rope", "k_norm_nope", "k_norm_rope", "w_mla_proj",
           "w_out", "norm_ffn2", "w_ffn2_in", "w_ffn2_out")
SMALL = ("b_ada", "norm_ffn1", "norm_mix", "pool_grp", "pool_scale", "q_a_norm", "kv_a_norm", "q_norm_nope", "q_norm_rope",
         "k_norm_nope", "k_norm_rope", "norm_ffn2")


def _assemble(name, stacked):
    if name in ROW_SHARDED:
        return stacked.reshape(stacked.shape[0] * stacked.shape[1], stacked.shape[2])
    return jnp.transpose(stacked, (1, 0, 2)).reshape(stacked.shape[1], stacked.shape[0] * stacked.shape[2])


def _split(name, full):
    if name in ROW_SHARDED:
        return full.reshape(N_CHIP, full.shape[0] // N_CHIP, full.shape[1])
    return jnp.transpose(full.reshape(full.shape[0], N_CHIP, full.shape[1] // N_CHIP), (1, 0, 2))


GU = ("w_ffn1_in", "w_ffn2_in")
NARROW = ("w_in", "w_q_up")
TRANSPOSED = ("w_in",)
W_IN_SHARD = (Z_W - LANE + ROPE) // N_CHIP


MIX_SMALL = ("w_out", "w_pool_proj", "w_mla_proj", "w_q_up", "w_kv_up")
RIDES = {
    "ada_prologue": (("gather", ("w_ffn1_in",)),),
    "ffn1_up_a": (("gather", ("w_ffn1_out",)),),
    "ffn1_up_b": (("gather", ("w_in",)),),
    "mix_in": (("gather", MIX_SMALL),),
    "attn_fwd": (("gather", ("w_ffn2_in", "w_ffn2_out")),),
    "d_ffn2_h": (("swap", ("w_ffn2_out", "w_ffn2_in")),),
    "attn_bwd": (("exchange", ("w_ffn2_out", "w_ffn2_in")),),
    "d_mix_in": (("swap", MIX_SMALL),),
    "d_mix_h": (("exchange", MIX_SMALL), ("swap", ("w_in",))),
    "ffn1_dact": (("exchange", ("w_in",)), ("swap", ("w_ffn1_out",))),
    "d_ffn1_in_a": (("exchange", ("w_ffn1_out",)),),
    "d_ffn1_in_b": (("swap", ("w_ffn1_in@0",)),),
    "d_ffn1_h": (("exchange", ("w_ffn1_in@0",)), ("swap", ("w_ffn1_in@1",))),
    "bwd_norm1": (("exchange", ("w_ffn1_in@1",)),),
    "small_allreduce": (("join", tuple(n for n in BIG if n != "w_ffn1_in") + ("w_ffn1_in@0", "w_ffn1_in@1")),),
}


def _kname(n):
    base, _, part = n.partition("@")
    return KERNEL_NAME[base] + ("@" + part if part else "")


def _both(comms):
    if len(comms) == 1:
        return comms[0]
    ins, outs, sems, aliases, spans = [], [], [], {}, []
    for c in comms:
        spans.append((len(ins), len(c.ins), len(outs), len(c.out_shapes), len(sems), len(c.sems)))
        aliases.update({len(ins) + i: len(outs) + o for i, o in c.aliases.items()})
        ins, outs, sems = ins + c.ins, outs + c.out_shapes, sems + c.sems

    def each(which):
        def run(i_, o_, s_):
            for c, (ia, ni, oa, no, sa, ns) in zip(comms, spans):
                getattr(c, which)(i_[ia:ia + ni], o_[oa:oa + no], s_[sa:sa + ns])
        return run

    return _Comm(ins, outs, sems, each("start"), each("finish"), aliases)


class _ExchangePlan:
    def __init__(self, shards, cidx):
        self.shards, self.cidx = shards, cidx
        self.W, self.G, self.parts, self.pre, self.reduced, self.joined = {}, {}, {}, {}, {}, {}

    def grad(self, key, g):
        self.G[key] = g

    def rider(self, name):
        comms = []
        for kind, names in RIDES.get(name, ()):
            if kind == "gather":
                comms.append(_gather_comm([self.shards[n] for n in names]))
            elif kind == "swap":
                for n in names:
                    self.parts[n] = self._stacked(n)
                comms.append(_swap_comm([self.parts[n] for n in names]))
            elif kind == "exchange":
                comms.append(_exchange_comm([self.pre[n] for n in names]))
            else:
                comms.append(_join_comm([self.reduced[n] for n in names]))
        return _both(comms) if comms else None

    def landed(self, name, outs):
        at = 0
        for kind, names in RIDES[name]:
            for n, o in zip(names, outs[at:at + len(names)]):
                if kind == "gather":
                    self.W[KERNEL_NAME[n]] = self._to_kernel(n, o)
                elif kind == "swap":
                    self.pre[n] = _add_half(self.parts[n], o, self.cidx, name="rs_add_" + _kname(n))
                elif kind == "exchange":
                    self.reduced[n] = _sum_chips(o, self.cidx, name="rs_sum_" + _kname(n))
                else:
                    self.joined[n] = o
            at += len(names)

    @staticmethod
    def _to_kernel(n, stacked):
        if n in GU:
            return stacked
        if n in TRANSPOSED:
            return _w_in_to_kernel(stacked[:, :W_IN_SHARD].reshape(N_CHIP * W_IN_SHARD, stacked.shape[2]))
        full = _assemble(n, stacked)
        return {"w_q_up": _q_up_to_kernel, "w_kv_up": _kv_up_to_kernel}.get(n, lambda w: w)(full)

    def _stacked(self, n):
        g = self.G[_kname(n)]
        if n.partition("@")[0] in GU:
            return g
        if n in TRANSPOSED:
            return _pad_rows(_w_in_from_kernel(g).reshape(N_CHIP, W_IN_SHARD, g.shape[1]))
        full = {"w_q_up": _q_up_from_kernel, "w_kv_up": _kv_up_from_kernel}.get(n, lambda w: w)(g)
        return _split(n, full)


def kernel(x, c, positions, w_ada, b_ada, norm_ffn1, w_ffn1_in, w_ffn1_out, norm_mix, w_in, pool_grp, pool_scale, w_pool_proj, q_a_norm, w_q_up, kv_a_norm, w_kv_up, q_norm_nope, q_norm_rope, k_norm_nope, k_norm_rope, w_mla_proj, w_out, norm_ffn2, w_ffn2_in, w_ffn2_out, loss_target, m_w_ada, m_b_ada, m_norm_ffn1, m_w_ffn1_in, m_w_ffn1_out, m_norm_mix, m_w_in, m_pool_grp, m_pool_scale, m_w_pool_proj, m_q_a_norm, m_w_q_up, m_kv_a_norm, m_w_kv_up, m_q_norm_nope, m_q_norm_rope, m_k_norm_nope, m_k_norm_rope, m_w_mla_proj, m_w_out, m_norm_ffn2, m_w_ffn2_in, m_w_ffn2_out, v_w_ada, v_b_ada, v_norm_ffn1, v_w_ffn1_in, v_w_ffn1_out, v_norm_mix, v_w_in, v_pool_grp, v_pool_scale, v_w_pool_proj, v_q_a_norm, v_w_q_up, v_kv_a_norm, v_w_kv_up, v_q_norm_nope, v_q_norm_rope, v_k_norm_nope, v_k_norm_rope, v_w_mla_proj, v_w_out, v_norm_ffn2, v_w_ffn2_in, v_w_ffn2_out):
    args = dict(locals())
    wts = {n: args[n][0] for n in WEIGHTS}
    mom = {n: args["m_" + n][0] for n in WEIGHTS}
    var = {n: args["v_" + n][0] for n in WEIGHTS}
    nb, seq, dm = x.shape
    tokens = nb * seq
    xi, yi, ci = _coords()
    chip = 2 * xi + yi

    cidx = ci.astype(jnp.int32).reshape(1)
    plan = _ExchangePlan({n: _pad_rows(wts[n].T.astype(MX)) if n in TRANSPOSED else wts[n].astype(MX) for n in BIG}, cidx)
    plan.W["pool_grp"] = wts["pool_grp"].astype(MX)

    ncol = w_ada.shape[2]
    b_cols = lax.dynamic_slice_in_dim(wts["b_ada"].reshape(1, -1), chip * ncol, ncol, axis=1)
    c_slots, mod_slots = _run(plan, _ada_prologue, c, wts["w_ada"], b_cols, name="ada_prologue")
    c_all = c_slots[:, :nb].reshape(N_DEV * nb, dm)
    mod3 = jnp.transpose(mod_slots[:, :nb], (1, 0, 2)).reshape(nb, 9, dm)
    P = {"norm_ffn1": wts["norm_ffn1"].reshape(1, dm), "norm_mix": wts["norm_mix"].reshape(1, dm),
         "norm_ffn2": wts["norm_ffn2"].reshape(1, dm), "pool_scale": wts["pool_scale"].reshape(1, POOL_W),
         "q_a_norm": wts["q_a_norm"].reshape(1, QL), "kv_a_norm": wts["kv_a_norm"].reshape(1, KVL),
         "q_gain": _gain_slab(wts["q_norm_nope"].reshape(1, NOPE), wts["q_norm_rope"].reshape(1, ROPE)),
         "k_gain": _gain_slab(wts["k_norm_nope"].reshape(1, NOPE), wts["k_norm_rope"].reshape(1, ROPE))}
    cos, sin = _rope_tables(positions.reshape(tokens))

    loss8, grad_x, stats, d_pool_grp = _layer_fwd_bwd(x.reshape(tokens, dm), loss_target.reshape(tokens, dm), mod3, cos, sin, plan, P)

    slab = _pack_stats(*stats, loss8, name="pack_stats")
    red, dmod_rows, pool_total = _run(plan, _small_allreduce, slab, d_pool_grp.reshape(POOL_G * LANE, LANE), name="small_allreduce")
    grads_t = {n: plan.joined[n][:W_IN_SHARD] for n in TRANSPOSED}
    grads = {n: grads_t[n].T if n in TRANSPOSED else plan.joined[n] for n in BIG if n != "w_ffn1_in"}
    grads["w_ffn1_in"] = jnp.concatenate([plan.joined["w_ffn1_in@0"], plan.joined["w_ffn1_in@1"]], axis=0)
    dm_rows = dmod_rows.shape[0] // N_DEV
    dmod_all = dmod_rows.reshape(N_DEV, dm_rows, SLAB_W)[:, :9 * nb].reshape(N_DEV * nb, 9 * dm)
    dmod_cols = lax.dynamic_slice_in_dim(dmod_all, chip * ncol, ncol, axis=1)
    grads["w_ada"] = _ada_bwd(c_all, dmod_cols, name="ada_bwd")

    delta, new_m, new_v = {}, {}, {}
    as2d = lambda a: a.reshape(POOL_G * LANE, LANE) if a.ndim == 4 else a.reshape(1, -1)
    upd, loss_row = _small_update(red, dmod_rows, pool_total, nb,
                                  {n: tuple(as2d(args[p + n]) for p in ("", "m_", "v_")) for n in SMALL}, name="small_update")
    loss = loss_row[0, 0]
    for n in SMALL:
        grads[n], delta[n], new_m[n], new_v[n] = upd[n]
    for n in ("w_ada",) + BIG:
        if n in NARROW:
            res = _adamw(wts[n].T, grads_t[n] if n in TRANSPOSED else grads[n].T, mom[n].T, var[n].T, name="adamw_" + n)
            delta[n], new_m[n], new_v[n] = (r.T for r in res)
        else:
            delta[n], new_m[n], new_v[n] = _adamw(wts[n], grads[n], mom[n], var[n], name="adamw_" + n)

    def lead(a, n):
        return a.reshape((1,) + wts[n].shape)

    return (loss, grad_x.reshape(nb, seq, dm), *[lead(grads[n], n) for n in WEIGHTS], *[lead(delta[n], n) for n in WEIGHTS],
            *[lead(new_m[n], n) for n in WEIGHTS], *[lead(new_v[n], n) for n in WEIGHTS])
```

```python
import functools
import math

import jax
import jax.numpy as jnp
from jax import lax
from jax.experimental import pallas as pl
from jax.experimental.pallas import tpu as pltpu

F32 = jnp.float32
MX = jnp.bfloat16

D = 1024
DFF = 2816
NH = 8
POOL_W = 512
POOL_G = 4
QL = 384
KVL = 256
ROPE = 32
NOPE = 64
LANE = 128
SUBLANE = 8
EPS = 1e-6
ATTN_SCALE = 1.0 / math.sqrt(96.0)
NEG = -1e30

Z_UP, Z_QL, Z_KV, Z_KR, Z_GP, Z_GM, Z_W = 0, 512, 896, 1152, 1280, 2304, 3328
KR_LANE = 64
LAT_W = 1280

ADAM_LR, ADAM_B1, ADAM_B2, ADAM_EPS, ADAM_WD, ADAM_STEP = 0.001, 0.9, 0.999, 1e-08, 0.01, 10

VMEM_LIMIT = 48 * 1024 * 1024
MESH = pl.DeviceIdType.MESH
N_DEV = 8
N_CHIP = 4


class _Comm:
    def __init__(self, ins, out_shapes, sems, start, finish, aliases=None):
        self.ins, self.out_shapes, self.sems = list(ins), list(out_shapes), list(sems)
        self.start, self.finish = start, finish
        self.aliases = aliases or {}


def _pcall(body, *, name, out_shape, grid=(), in_specs=None, out_specs=None, scratch=(), grid_spec=None, aliases=None,
           comm=None):
    params = pltpu.CompilerParams(vmem_limit_bytes=VMEM_LIMIT)
    kw = dict(name=name, compiler_params=params)
    if comm is None:
        if aliases:
            kw["input_output_aliases"] = aliases
        if grid_spec is not None:
            return pl.pallas_call(body, grid_spec=grid_spec, out_shape=out_shape, **kw)
        return pl.pallas_call(body, grid=grid, in_specs=in_specs, out_specs=out_specs, scratch_shapes=scratch,
                              out_shape=out_shape, **kw)
    single = not isinstance(out_shape, (list, tuple))
    outs = [out_shape] if single else list(out_shape)
    ospecs = [out_specs] if single else list(out_specs)
    n_in, n_out, n_ci, n_co, n_scr = len(in_specs), len(outs), len(comm.ins), len(comm.out_shapes), len(scratch)
    io = dict(aliases or {})
    io.update({n_in + i: n_out + o for i, o in comm.aliases.items()})

    def riding(*refs):
        ins, cins = refs[:n_in], refs[n_in:n_in + n_ci]
        at = n_in + n_ci
        os_, couts = refs[at:at + n_out], refs[at + n_out:at + n_out + n_co]
        at += n_out + n_co
        scr, csems = refs[at:at + n_scr], refs[at + n_scr:]
        if grid:
            first = functools.reduce(jnp.logical_and, [pl.program_id(d) == 0 for d in range(len(grid))])
            last = functools.reduce(jnp.logical_and, [pl.program_id(d) == grid[d] - 1 for d in range(len(grid))])
            pl.when(first)(lambda: comm.start(cins, couts, csems))
            body(*ins, *os_, *scr)
            pl.when(last)(lambda: comm.finish(cins, couts, csems))
        else:
            comm.start(cins, couts, csems)
            body(*ins, *os_, *scr)
            comm.finish(cins, couts, csems)

    call = pl.pallas_call(riding, grid=grid, in_specs=list(in_specs) + [HBM_SPEC] * n_ci, out_specs=ospecs + [HBM_SPEC] * n_co,
                          out_shape=outs + comm.out_shapes, scratch_shapes=list(scratch) + comm.sems,
                          input_output_aliases=io, **kw)

    def run(*args):
        res = call(*args, *comm.ins)
        main = list(res[:n_out])
        return (main[0] if single else main), list(res[n_out:])

    return run


def _pick(dim, target):
    best = None
    for t in range(LANE, min(dim, target) + 1, LANE):
        if dim % t == 0:
            best = t
    return dim if best is None else best


def _sds(shape, dtype):
    return jax.ShapeDtypeStruct(shape, dtype)


def _dw(a, g, *, name, tm=512, tn=1024, out_t=False, comm=None):
    return _mm(a, g, mode="tn", out_dtype=F32, name=name, tm=tm, tn=tn, tk=a.shape[0], n_outer=True, out_t=out_t, comm=comm)


def _mm(a, b, *, mode, out_dtype, name, tm=1024, tn=1024, tk=4096, n_outer=False, out_t=False, comm=None):
    if mode == "nn":
        (M, K), (K2, N) = a.shape, b.shape
    elif mode == "nt":
        (M, K), (N, K2) = a.shape, b.shape
    else:
        (K, M), (K2, N) = a.shape, b.shape
    assert K == K2, (name, a.shape, b.shape)
    tm, tn, tk = _pick(M, tm), _pick(N, tn), _pick(K, tk)
    nk = K // tk
    if n_outer:
        ij = lambda g0, g1: (g1, g0)
        grid = (N // tn, M // tm, nk)
    else:
        ij = lambda g0, g1: (g0, g1)
        grid = (M // tm, N // tn, nk)
    if mode == "tn":
        a_spec = pl.BlockSpec((tk, tm), lambda g0, g1, k: (k, ij(g0, g1)[0]))
    else:
        a_spec = pl.BlockSpec((tm, tk), lambda g0, g1, k: (ij(g0, g1)[0], k))
    if mode == "nt":
        b_spec = pl.BlockSpec((tn, tk), lambda g0, g1, k: (ij(g0, g1)[1], k))
    else:
        b_spec = pl.BlockSpec((tk, tn), lambda g0, g1, k: (k, ij(g0, g1)[1]))
    if out_t:
        assert nk == 1, name
        o_spec = pl.BlockSpec((tn, tm), lambda g0, g1, k: ij(g0, g1)[::-1])
    else:
        o_spec = pl.BlockSpec((tm, tn), lambda g0, g1, k: ij(g0, g1))
    dn = {"nn": (((1,), (0,)), ((), ())), "nt": (((1,), (1,)), ((), ())), "tn": (((0,), (0,)), ((), ()))}[mode]

    def dot(a_ref, b_ref):
        return lax.dot_general(a_ref[...].astype(MX), b_ref[...].astype(MX), dn, preferred_element_type=F32)

    def body_one(a_ref, b_ref, o_ref):
        prod = dot(a_ref, b_ref)
        o_ref[...] = (prod.T if out_t else prod).astype(o_ref.dtype)

    def body_acc(a_ref, b_ref, o_ref, acc_ref):
        k = pl.program_id(2)
        part = dot(a_ref, b_ref)

        @pl.when(k == 0)
        def _():
            acc_ref[...] = part

        @pl.when(k > 0)
        def _():
            acc_ref[...] += part

        @pl.when(k == nk - 1)
        def _():
            o_ref[...] = acc_ref[...].astype(o_ref.dtype)

    return _pcall(body_one if nk == 1 else body_acc, name=name, out_shape=_sds((N, M) if out_t else (M, N), out_dtype), grid=grid,
                  in_specs=[a_spec, b_spec], out_specs=o_spec, scratch=[] if nk == 1 else [pltpu.VMEM((tm, tn), F32)],
                  comm=comm)(a, b)


def _gu_shard(q):
    return (q % 2) * 2 + q // 2


def _ffn_up(h, w_st, *, name, tm=512, comm=None):
    T, dm = h.shape
    hw = w_st.shape[2]
    tm = _pick(T, tm)

    def body(h_ref, wg_ref, wu_ref, gu_ref, a_ref):
        hv = h_ref[...]
        g = jnp.dot(hv, wg_ref[0], preferred_element_type=F32)
        u = jnp.dot(hv, wu_ref[0], preferred_element_type=F32)
        gu_ref[:, :hw] = g.astype(gu_ref.dtype)
        gu_ref[:, hw:] = u.astype(gu_ref.dtype)
        a_ref[...] = (g * _sigmoid(g) * u).astype(a_ref.dtype)

    return _pcall(body, name=name, grid=(T // tm, 2),
                  in_specs=[pl.BlockSpec((tm, dm), lambda i, j: (i, 0)), pl.BlockSpec((1, dm, hw), lambda i, j: (j, 0, 0)),
                            pl.BlockSpec((1, dm, hw), lambda i, j: (2 + j, 0, 0))],
                  out_specs=[pl.BlockSpec((tm, 2 * hw), lambda i, j: (i, j)), pl.BlockSpec((tm, hw), lambda i, j: (i, j))],
                  out_shape=[_sds((T, 4 * hw), MX), _sds((T, 2 * hw), MX)], comm=comm)(h, w_st, w_st)


def _ffn_up_first(x, mod3, gain, w_st, *, sub, name, tm=512, comm=None):
    T, dm = x.shape
    hw = w_st.shape[2]
    tm = _pick(T, tm)
    tps = (T // mod3.shape[0]) // tm

    def body(x_ref, mod_ref, n_ref, wg_ref, wu_ref, h_ref, gu_ref, a_ref):
        xv = x_ref[...]
        xn = xv * _rsq(xv) * n_ref[...]
        hv = (xn * (1.0 + mod_ref[0, 3 * sub + 1:3 * sub + 2, :]) + mod_ref[0, 3 * sub:3 * sub + 1, :]).astype(h_ref.dtype)
        h_ref[...] = hv
        g = jnp.dot(hv, wg_ref[0], preferred_element_type=F32)
        u = jnp.dot(hv, wu_ref[0], preferred_element_type=F32)
        gu_ref[:, :hw] = g.astype(gu_ref.dtype)
        gu_ref[:, hw:] = u.astype(gu_ref.dtype)
        a_ref[...] = (g * _sigmoid(g) * u).astype(a_ref.dtype)

    return _pcall(body, name=name, grid=(T // tm,),
                  in_specs=[_row_spec(tm, dm), pl.BlockSpec((1, 9, dm), lambda i: (i // tps, 0, 0)),
                            pl.BlockSpec((1, dm), lambda i: (0, 0)), pl.BlockSpec((1, dm, hw), lambda i: (0, 0, 0)),
                            pl.BlockSpec((1, dm, hw), lambda i: (2, 0, 0))],
                  out_specs=[_row_spec(tm, dm), pl.BlockSpec((tm, 2 * hw), lambda i: (i, 0)), pl.BlockSpec((tm, hw), lambda i: (i, 0))],
                  out_shape=[_sds((T, dm), MX), _sds((T, 4 * hw), MX), _sds((T, 2 * hw), MX)],
                  comm=comm)(x, mod3, gain, w_st, w_st)


def _ffn_up_second(h, w_st, gu, a, *, name, tm=512, comm=None):
    T, dm = h.shape
    hw = w_st.shape[2]
    tm = _pick(T, tm)

    def body(h_ref, wg_ref, wu_ref, gu_in, a_in, gu_ref, a_ref):
        hv = h_ref[...]
        g = jnp.dot(hv, wg_ref[0], preferred_element_type=F32)
        u = jnp.dot(hv, wu_ref[0], preferred_element_type=F32)
        gu_ref[:, :hw] = g.astype(gu_ref.dtype)
        gu_ref[:, hw:] = u.astype(gu_ref.dtype)
        a_ref[...] = (g * _sigmoid(g) * u).astype(a_ref.dtype)

    return _pcall(body, name=name, grid=(T // tm,),
                  in_specs=[_row_spec(tm, dm), pl.BlockSpec((1, dm, hw), lambda i: (1, 0, 0)),
                            pl.BlockSpec((1, dm, hw), lambda i: (3, 0, 0)), HBM_SPEC, HBM_SPEC],
                  out_specs=[pl.BlockSpec((tm, 2 * hw), lambda i: (i, 1)), pl.BlockSpec((tm, hw), lambda i: (i, 1))],
                  out_shape=[_sds(gu.shape, gu.dtype), _sds(a.shape, a.dtype)], aliases={3: 0, 4: 1},
                  comm=comm)(h, w_st, w_st, gu, a)


def _ffn_dact(df, gu, w_out, *, name, tm=512, comm=None):
    T, dm = df.shape
    hw = gu.shape[1] // 4
    tm = _pick(T, tm)

    def body(df_ref, gu_ref, wo_ref, dgu_ref):
        da = lax.dot_general(df_ref[...], wo_ref[...], (((1,), (1,)), ((), ())), preferred_element_type=F32)
        g = gu_ref[:, :hw].astype(F32)
        u = gu_ref[:, hw:].astype(F32)
        s = _sigmoid(g)
        dgu_ref[:, :hw] = (da * u * (s * (1.0 + g * (1.0 - s)))).astype(dgu_ref.dtype)
        dgu_ref[:, hw:] = (da * (g * s)).astype(dgu_ref.dtype)

    return _pcall(body, name=name, grid=(T // tm, 2),
                  in_specs=[pl.BlockSpec((tm, dm), lambda i, j: (i, 0)), pl.BlockSpec((tm, 2 * hw), lambda i, j: (i, j)),
                            pl.BlockSpec((hw, dm), lambda i, j: (j, 0))],
                  out_specs=pl.BlockSpec((tm, 2 * hw), lambda i, j: (i, j)), out_shape=_sds(gu.shape, MX),
                  comm=comm)(df, gu, w_out)


def _ffn_dh(dgu, w_st, *, name, tm=1024, comm=None):
    T = dgu.shape[0]
    _, dm, hw = w_st.shape
    tm = _pick(T, tm)

    def body(d_ref, w_ref, o_ref, acc_ref):
        q = pl.program_id(1)
        part = lax.dot_general(d_ref[...], w_ref[0], (((1,), (1,)), ((), ())), preferred_element_type=F32)

        @pl.when(q == 0)
        def _():
            acc_ref[...] = part

        @pl.when(jnp.logical_and(q > 0, q < 3))
        def _():
            acc_ref[...] += part

        @pl.when(q == 3)
        def _():
            o_ref[...] = acc_ref[...] + part

    return _pcall(body, name=name, grid=(T // tm, 4),
                  in_specs=[pl.BlockSpec((tm, hw), lambda i, q: (i, q)), pl.BlockSpec((1, dm, hw), lambda i, q: (_gu_shard(q), 0, 0))],
                  out_specs=pl.BlockSpec((tm, dm), lambda i, q: (i, 0)), out_shape=_sds((T, dm), F32),
                  scratch=[pltpu.VMEM((tm, dm), F32)], comm=comm)(dgu, w_st)


def _ffn_dw_in(h, dgu, *, name, rows=None, tm=512, comm=None):
    T, dm = h.shape
    hw = dgu.shape[1] // 4
    first, count = rows if rows is not None else (0, dm)
    tm = _pick(count, tm)
    skip = first // tm

    def body(h_ref, d_ref, o_ref):
        o_ref[0] = lax.dot_general(h_ref[...], d_ref[...], (((0,), (0,)), ((), ())), preferred_element_type=F32)

    return _pcall(body, name=name, grid=(4, count // tm),
                  in_specs=[pl.BlockSpec((T, tm), lambda q, i: (0, skip + i)), pl.BlockSpec((T, hw), lambda q, i: (0, q))],
                  out_specs=pl.BlockSpec((1, tm, hw), lambda q, i: (_gu_shard(q), i, 0)),
                  out_shape=_sds((4, count, hw), F32), comm=comm)(h, dgu)


def _rsq(x):
    return lax.rsqrt(jnp.mean(x * x, axis=-1, keepdims=True) + EPS)


def _sigmoid(x):
    return 1.0 / (1.0 + jnp.exp(-x))


def _row_spec(tm, w):
    return pl.BlockSpec((tm, w), lambda i: (i, 0))


def _mm_resid_norm(a, w, x_prev, mod3, gain, *, sub, coef, name, tm=512, comm=None):
    T, k = a.shape
    dm = w.shape[1]
    tm = _pick(T, tm)
    tps = (T // mod3.shape[0]) // tm

    def body(a_ref, w_ref, x_ref, mod_ref, n_ref, f_ref, xo_ref, h_ref):
        f = jnp.dot(a_ref[...], w_ref[...], preferred_element_type=F32)
        f_ref[...] = f
        x = x_ref[...] + coef * mod_ref[0, 3 * sub - 1:3 * sub, :] * f
        xo_ref[...] = x
        xn = x * _rsq(x) * n_ref[...]
        h_ref[...] = (xn * (1.0 + mod_ref[0, 3 * sub + 1:3 * sub + 2, :]) + mod_ref[0, 3 * sub:3 * sub + 1, :]).astype(h_ref.dtype)

    row = _row_spec(tm, dm)
    return _pcall(body, name=name, grid=(T // tm,),
                  in_specs=[_row_spec(tm, k), pl.BlockSpec((k, dm), lambda i: (0, 0)), row,
                            pl.BlockSpec((1, 9, dm), lambda i: (i // tps, 0, 0)), pl.BlockSpec((1, dm), lambda i: (0, 0))],
                  out_specs=[row, row, row], out_shape=[_sds((T, dm), F32), _sds((T, dm), F32), _sds((T, dm), MX)],
                  comm=comm)(a, w, x_prev, mod3, gain)


def _mm_loss(a, w, x2, tgt, mod3, *, name, tm=512):
    T, k = a.shape
    dm = w.shape[1]
    tm = _pick(T, tm)
    tps = (T // mod3.shape[0]) // tm
    mod_spec = pl.BlockSpec((1, 9, dm), lambda i: (i // tps, 0, 0))
    row = _row_spec(tm, dm)
    stat_spec = pl.BlockSpec((1, SUBLANE, dm), lambda i: (i // tps, 0, 0))
    loss_spec = pl.BlockSpec((SUBLANE, LANE), lambda i: (0, 0))

    def body(a_ref, w_ref, x_ref, t_ref, mod_ref, dy_ref, df_ref, st_ref, loss_ref):
        i = pl.program_id(0)
        g = mod_ref[0, 8:9, :]
        f = jnp.dot(a_ref[...], w_ref[...], preferred_element_type=F32)
        err = x_ref[...] + 0.5 * g * f - t_ref[...]
        dy = err * (1.0 / dm)
        dy_ref[...] = dy
        df_ref[...] = (0.5 * g * dy).astype(df_ref.dtype)
        dgate = jnp.sum(0.5 * dy * f, axis=0, keepdims=True)
        part = 0.5 * jnp.sum(jnp.sum(err * err, axis=0, keepdims=True), axis=1, keepdims=True) * (1.0 / dm)

        @pl.when(i % tps == 0)
        def _():
            st_ref[...] = jnp.zeros_like(st_ref)

        @pl.when(i == 0)
        def _():
            loss_ref[...] = jnp.zeros_like(loss_ref)

        st_ref[0, 0:1, :] += dgate
        loss_ref[...] += jnp.broadcast_to(part, loss_ref.shape)

    return _pcall(body, name=name, grid=(T // tm,),
                  in_specs=[_row_spec(tm, k), pl.BlockSpec((k, dm), lambda i: (0, 0)), row, row, mod_spec],
                  out_specs=[row, row, stat_spec, loss_spec],
                  out_shape=[_sds((T, dm), F32), _sds((T, dm), MX), _sds((mod3.shape[0], SUBLANE, dm), F32),
                             _sds((SUBLANE, LANE), F32)])(a, w, x2, tgt, mod3)


def _norm_bwd_tail(dhv, x_ref, dxi_ref, f_ref, mod_ref, n_ref, dx_ref, df_ref, st_ref, *, first, sub, coef):
    x = x_ref[...]
    r = _rsq(x)
    xhat = x * r
    n = n_ref[...]
    d_shift = jnp.sum(dhv, axis=0, keepdims=True)
    d_scale = jnp.sum(dhv * (xhat * n), axis=0, keepdims=True)
    dxn = dhv * (1.0 + mod_ref[0, 3 * sub + 1:3 * sub + 2, :])
    d_gain = jnp.sum(dxn * xhat, axis=0, keepdims=True)
    dxhat = dxn * n
    dx = dxi_ref[...] + r * (dxhat - xhat * jnp.mean(dxhat * xhat, axis=-1, keepdims=True))
    dx_ref[...] = dx

    @pl.when(first)
    def _():
        st_ref[...] = jnp.zeros_like(st_ref)

    st_ref[0, 0:1, :] += d_shift
    st_ref[0, 1:2, :] += d_scale
    st_ref[0, 2:3, :] += d_gain
    if f_ref is not None:
        st_ref[0, 3:4, :] += jnp.sum(coef * dx * f_ref[...], axis=0, keepdims=True)
        df_ref[...] = (coef * mod_ref[0, 3 * sub - 1:3 * sub, :] * dx).astype(df_ref.dtype)


def _ffn_dh_norm(dgu, w_st, x_cur, dx_in, f_prev, mod3, gain, *, sub, coef, name, tm=512, comm=None):
    T = dgu.shape[0]
    _, dm, hw = w_st.shape
    tm = _pick(T, tm)
    nb = mod3.shape[0]
    tps = (T // nb) // tm

    def body(d_ref, w_ref, x_ref, dxi_ref, f_ref, mod_ref, n_ref, dx_ref, df_ref, st_ref, acc_ref):
        i = pl.program_id(0)
        q = pl.program_id(1)
        part = lax.dot_general(d_ref[...], w_ref[0], (((1,), (1,)), ((), ())), preferred_element_type=F32)

        @pl.when(q == 0)
        def _():
            acc_ref[...] = part

        @pl.when(q > 0)
        def _():
            acc_ref[...] += part

        @pl.when(q == 3)
        def _():
            _norm_bwd_tail(acc_ref[...], x_ref, dxi_ref, f_ref, mod_ref, n_ref, dx_ref, df_ref, st_ref,
                           first=i % tps == 0, sub=sub, coef=coef)

    row = pl.BlockSpec((tm, dm), lambda i, q: (i, 0))
    return _pcall(body, name=name, grid=(T // tm, 4),
                  in_specs=[pl.BlockSpec((tm, hw), lambda i, q: (i, q)), pl.BlockSpec((1, dm, hw), lambda i, q: (_gu_shard(q), 0, 0)),
                            row, row, row, pl.BlockSpec((1, 9, dm), lambda i, q: (i // tps, 0, 0)),
                            pl.BlockSpec((1, dm), lambda i, q: (0, 0))],
                  out_specs=[row, row, pl.BlockSpec((1, SUBLANE, dm), lambda i, q: (i // tps, 0, 0))],
                  out_shape=[_sds((T, dm), F32), _sds((T, dm), MX), _sds((nb, SUBLANE, dm), F32)],
                  scratch=[pltpu.VMEM((tm, dm), F32)], comm=comm)(dgu, w_st, x_cur, dx_in, f_prev, mod3, gain)


def _mm_norm_bwd(a, b, x_cur, dx_in, f_prev, mod3, gain, *, sub, coef, name, tm=512, comm=None):
    T, k = a.shape
    dm = b.shape[1]
    tm = _pick(T, tm)
    nb = mod3.shape[0]
    tps = (T // nb) // tm

    def body(a_ref, b_ref, x_ref, dxi_ref, f_ref, mod_ref, n_ref, dx_ref, df_ref, st_ref):
        dh = jnp.dot(a_ref[...], b_ref[...], preferred_element_type=F32)
        _norm_bwd_tail(dh, x_ref, dxi_ref, f_ref, mod_ref, n_ref, dx_ref, df_ref, st_ref,
                       first=pl.program_id(0) % tps == 0, sub=sub, coef=coef)

    row = _row_spec(tm, dm)
    return _pcall(body, name=name, grid=(T // tm,),
                  in_specs=[_row_spec(tm, k), pl.BlockSpec((k, dm), lambda i: (0, 0)), row, row, row,
                            pl.BlockSpec((1, 9, dm), lambda i: (i // tps, 0, 0)), pl.BlockSpec((1, dm), lambda i: (0, 0))],
                  out_specs=[row, row, pl.BlockSpec((1, SUBLANE, dm), lambda i: (i // tps, 0, 0))],
                  out_shape=[_sds((T, dm), F32), _sds((T, dm), MX), _sds((nb, SUBLANE, dm), F32)],
                  comm=comm)(a, b, x_cur, dx_in, f_prev, mod3, gain)


def _bwd_block(x_cur, dh, dx_in, f_prev, mod3, gain, *, sub, coef, name, tm=256, comm=None):
    T, dm = x_cur.shape
    nb = mod3.shape[0]
    tps = (T // nb) // tm
    has_f = f_prev is not None
    mod_spec = pl.BlockSpec((1, 9, dm), lambda i: (i // tps, 0, 0))
    vec_spec = pl.BlockSpec((1, dm), lambda i: (0, 0))
    stat_spec = pl.BlockSpec((1, SUBLANE, dm), lambda i: (i // tps, 0, 0))
    row = _row_spec(tm, dm)

    def body(*refs):
        if has_f:
            x_ref, dh_ref, dxi_ref, f_ref, mod_ref, n_ref, dx_ref, df_ref, st_ref = refs
        else:
            x_ref, dh_ref, dxi_ref, mod_ref, n_ref, dx_ref, st_ref = refs
            f_ref = df_ref = None
        _norm_bwd_tail(dh_ref[...], x_ref, dxi_ref, f_ref, mod_ref, n_ref, dx_ref, df_ref, st_ref,
                       first=pl.program_id(0) % tps == 0, sub=sub, coef=coef)

    st_shape = _sds((nb, SUBLANE, dm), F32)
    if has_f:
        return _pcall(body, name=name, grid=(T // tm,), in_specs=[row, row, row, row, mod_spec, vec_spec],
                      out_specs=[row, row, stat_spec],
                      out_shape=[_sds((T, dm), F32), _sds((T, dm), MX), st_shape], comm=comm)(x_cur, dh, dx_in, f_prev, mod3, gain)
    return _pcall(body, name=name, grid=(T // tm,), in_specs=[row, row, row, mod_spec, vec_spec],
                  out_specs=[row, stat_spec], out_shape=[_sds((T, dm), F32), st_shape], comm=comm)(x_cur, dh, dx_in, mod3, gain)


def _mix_out_fwd(z, br_pool, attn, w_mla, w_out, x_prev, mod3, gain, *, sub, name, tm=512):
    T = z.shape[0]
    dm = w_out.shape[1]
    tm = _pick(T, tm)
    tps = (T // mod3.shape[0]) // tm
    whole = lambda a: pl.BlockSpec(a.shape, lambda i: (0, 0))

    def body(z_ref, bp_ref, at_ref, wm_ref, wo_ref, x_ref, mod_ref, n_ref, bm_ref, mg_ref, mo_ref, xo_ref, h_ref):
        bm = jnp.dot(at_ref[...], wm_ref[...], preferred_element_type=F32).astype(MX)
        bm_ref[...] = bm
        gp = z_ref[:, Z_GP:Z_GP + D].astype(F32)
        gm = z_ref[:, Z_GM:Z_GM + D].astype(F32)
        merged = (_sigmoid(gp) * bp_ref[...] + _sigmoid(gm) * bm).astype(MX)
        mg_ref[...] = merged
        mo = jnp.dot(merged, wo_ref[...], preferred_element_type=F32)
        mo_ref[...] = mo
        x = x_ref[...] + mod_ref[0, 3 * sub - 1:3 * sub, :] * mo
        xo_ref[...] = x
        xn = x * _rsq(x) * n_ref[...]
        h_ref[...] = (xn * (1.0 + mod_ref[0, 3 * sub + 1:3 * sub + 2, :]) + mod_ref[0, 3 * sub:3 * sub + 1, :]).astype(h_ref.dtype)

    row = _row_spec(tm, dm)
    return _pcall(body, name=name, grid=(T // tm,),
                  in_specs=[_row_spec(tm, Z_W), row, _row_spec(tm, attn.shape[1]), whole(w_mla), whole(w_out), row,
                            pl.BlockSpec((1, 9, dm), lambda i: (i // tps, 0, 0)), pl.BlockSpec((1, dm), lambda i: (0, 0))],
                  out_specs=[row, row, row, row, row],
                  out_shape=[_sds((T, dm), MX), _sds((T, dm), MX), _sds((T, dm), F32), _sds((T, dm), F32), _sds((T, dm), MX)],
                  )(z, br_pool, attn, w_mla, w_out, x_prev, mod3, gain)


def _mix_out_bwd(dmo, w_out, z, br_pool, br_mla, w_mla, *, name, tm=512):
    T = z.shape[0]
    tm = _pick(T, tm)
    whole = lambda a: pl.BlockSpec(a.shape, lambda i: (0, 0))
    nt = (((1,), (1,)), ((), ()))

    def body(dmo_ref, wo_ref, z_ref, bp_ref, bm_ref, wm_ref, dbp_ref, dbm_ref, dg_ref, dat_ref):
        dm = lax.dot_general(dmo_ref[...], wo_ref[...], nt, preferred_element_type=F32)
        sp = _sigmoid(z_ref[:, Z_GP:Z_GP + D].astype(F32))
        sm = _sigmoid(z_ref[:, Z_GM:Z_GM + D].astype(F32))
        dbp_ref[...] = (dm * sp).astype(dbp_ref.dtype)
        dbm = (dm * sm).astype(MX)
        dbm_ref[...] = dbm
        dg_ref[:, :D] = (dm * bp_ref[...].astype(F32) * sp * (1.0 - sp)).astype(dg_ref.dtype)
        dg_ref[:, D:] = (dm * bm_ref[...].astype(F32) * sm * (1.0 - sm)).astype(dg_ref.dtype)
        dat_ref[...] = lax.dot_general(dbm, wm_ref[...], nt, preferred_element_type=F32)

    row = _row_spec(tm, D)
    return _pcall(body, name=name, grid=(T // tm,),
                  in_specs=[row, whole(w_out), _row_spec(tm, Z_W), row, row, whole(w_mla)],
                  out_specs=[row, row, _row_spec(tm, 2 * D), _row_spec(tm, w_mla.shape[0])],
                  out_shape=[_sds((T, D), MX), _sds((T, D), MX), _sds((T, 2 * D), MX), _sds((T, w_mla.shape[0]), F32)],
                  )(dmo, w_out, z, br_pool, br_mla, w_mla)


def _shift_down(x, k, row):
    return jnp.where(row >= k, pltpu.roll(x, k, 0), 0.0)


def _shift_up(x, k, row, n):
    return jnp.where(row < n - k, pltpu.roll(x, n - k, 0), 0.0)


def _pool_fwd(z, pool_grp, pool_scale, *, nb, name):
    T = z.shape[0]
    S = T // nb
    blk = pl.BlockSpec((S, LANE), lambda b, g: (b, g))

    def body(u_ref, w_ref, s_ref, pooled_ref, mixed_ref, scaled_ref):
        g = pl.program_id(1)
        u = u_ref[...].astype(F32)
        row = lax.broadcasted_iota(jnp.int32, u.shape, 0)
        s2 = u + _shift_down(u, 1, row)
        s4 = s2 + _shift_down(s2, 2, row)
        s8 = s4 + _shift_down(s4, 4, row)
        s16 = s8 + _shift_down(s8, 8, row)
        win = jnp.where(g == 0, s2, jnp.where(g == 1, s4, jnp.where(g == 2, s8, s16)))
        width = lax.shift_left(jnp.int32(2), g)
        cnt = jnp.minimum(row + 1, width).astype(F32)
        pooled = (win / cnt - u).astype(MX)
        pooled_ref[...] = pooled
        mixed = jnp.dot(pooled, w_ref[0], preferred_element_type=F32)
        mixed_ref[...] = mixed
        scaled_ref[...] = (mixed * s_ref[...]).astype(scaled_ref.dtype)

    return _pcall(body, name=name, grid=(nb, POOL_G),
                  in_specs=[blk, pl.BlockSpec((1, LANE, LANE), lambda b, g: (g, 0, 0)),
                            pl.BlockSpec((1, LANE), lambda b, g: (0, g))],
                  out_specs=[blk, blk, blk],
                  out_shape=[_sds((T, POOL_W), MX), _sds((T, POOL_W), F32), _sds((T, POOL_W), MX)])(z, pool_grp, pool_scale)


def _pool_bwd(dscaled, mixed, pooled, pool_grp, pool_scale, *, nb, name):
    T = dscaled.shape[0]
    S = T // nb
    blk = pl.BlockSpec((S, LANE), lambda g, b: (b, g))

    def body(ds_ref, mixed_ref, pooled_ref, w_ref, s_ref, du_ref, dw_ref, dsc_ref):
        g = pl.program_id(0)
        b = pl.program_id(1)
        ds = ds_ref[...]
        dsc_ref[0] = jnp.sum(ds * mixed_ref[...], axis=0, keepdims=True)
        dmixed = (ds * s_ref[...]).astype(MX)
        dw = lax.dot_general(pooled_ref[...], dmixed, (((0,), (0,)), ((), ())), preferred_element_type=F32)

        @pl.when(b == 0)
        def _():
            dw_ref[0] = dw

        @pl.when(b > 0)
        def _():
            dw_ref[0] += dw
        dpooled = lax.dot_general(dmixed, w_ref[0], (((1,), (1,)), ((), ())), preferred_element_type=F32)
        row = lax.broadcasted_iota(jnp.int32, dpooled.shape, 0)
        width = lax.shift_left(jnp.int32(2), g)
        q = dpooled / jnp.minimum(row + 1, width).astype(F32)
        r2 = q + _shift_up(q, 1, row, S)
        r4 = r2 + _shift_up(r2, 2, row, S)
        r8 = r4 + _shift_up(r4, 4, row, S)
        r16 = r8 + _shift_up(r8, 8, row, S)
        win = jnp.where(g == 0, r2, jnp.where(g == 1, r4, jnp.where(g == 2, r8, r16)))
        du_ref[...] = (win - dpooled).astype(du_ref.dtype)

    return _pcall(body, name=name, grid=(POOL_G, nb),
                  in_specs=[blk, blk, blk, pl.BlockSpec((1, LANE, LANE), lambda g, b: (g, 0, 0)),
                            pl.BlockSpec((1, LANE), lambda g, b: (0, g))],
                  out_specs=[blk, pl.BlockSpec((1, LANE, LANE), lambda g, b: (g, 0, 0)),
                             pl.BlockSpec((1, 1, LANE), lambda g, b: (b, 0, g))],
                  out_shape=[_sds((T, POOL_W), MX), _sds((POOL_G, LANE, LANE), F32), _sds((nb, 1, POOL_W), F32)],
                  )(dscaled, mixed, pooled, pool_grp, pool_scale)


def _lane_masks(shape):
    lane = lax.broadcasted_iota(jnp.int32, shape, len(shape) - 1)
    m_n = lane < NOPE
    m_r = jnp.logical_and(lane >= KR_LANE, lane < KR_LANE + ROPE)
    first_half = lane < KR_LANE + ROPE // 2
    return m_n, m_r, first_half


def _rot(y, first_half):
    return jnp.where(first_half, -pltpu.roll(y, LANE - ROPE // 2, 1), pltpu.roll(y, ROPE // 2, 1))


def _rot_t(v, first_half, m_r):
    return jnp.where(m_r, jnp.where(first_half, pltpu.roll(v, LANE - ROPE // 2, 1), -pltpu.roll(v, ROPE // 2, 1)), 0.0)


def _mla_in_fwd(z, w_q, w_kv, qa_gain, kva_gain, cos, sin, q_gain, k_gain, *, name, tm=256):
    T = z.shape[0]
    slab = pl.BlockSpec((tm, LANE), lambda i: (i, 0))
    vec = pl.BlockSpec((1, LANE), lambda i: (0, 0))
    whole = lambda a: pl.BlockSpec(a.shape, lambda i: (0, 0))

    def body(z_ref, wq_ref, wkv_ref, qa_ref, kva_ref, cos_ref, sin_ref, qg_ref, kg_ref,
             qn_ref, kvn_ref, qp_ref, kvp_ref, q_ref, k_ref, v_ref):
        ql = z_ref[:, Z_QL:Z_QL + QL].astype(F32)
        kvl = z_ref[:, Z_KV:Z_KV + KVL].astype(F32)
        qn = (ql * _rsq(ql) * qa_ref[...]).astype(MX)
        kvn = (kvl * _rsq(kvl) * kva_ref[...]).astype(MX)
        qn_ref[...] = qn
        kvn_ref[...] = kvn
        qp = jnp.dot(qn, wq_ref[...], preferred_element_type=F32)
        kvp = jnp.dot(kvn, wkv_ref[...], preferred_element_type=F32)
        qp_ref[...] = qp
        kvp_ref[...] = kvp
        m_n, m_r, first_half = _lane_masks((tm, LANE))
        c = cos_ref[...]
        s = sin_ref[...]
        qg = qg_ref[...]
        kg = kg_ref[...]
        xr = z_ref[:, Z_KR:Z_KR + LANE].astype(F32)
        rr = lax.rsqrt(jnp.sum(xr * xr, axis=-1, keepdims=True) * (1.0 / ROPE) + EPS)
        yr = xr * rr * kg
        kr = jnp.where(m_r, yr * c + _rot(yr, first_half) * s, 0.0)
        for h in range(NH):
            x = qp[:, h * LANE:(h + 1) * LANE]
            x2 = x * x
            rn = lax.rsqrt(jnp.sum(jnp.where(m_n, x2, 0.0), axis=-1, keepdims=True) * (1.0 / NOPE) + EPS)
            rq = lax.rsqrt(jnp.sum(jnp.where(m_r, x2, 0.0), axis=-1, keepdims=True) * (1.0 / ROPE) + EPS)
            y = x * jnp.where(m_n, rn, jnp.where(m_r, rq, 0.0)) * qg
            q_ref[:, h * LANE:(h + 1) * LANE] = ((y * c + _rot(y, first_half) * s) * ATTN_SCALE).astype(q_ref.dtype)
            xk = kvp[:, h * LANE:(h + 1) * LANE]
            rk = lax.rsqrt(jnp.sum(jnp.where(m_n, xk * xk, 0.0), axis=-1, keepdims=True) * (1.0 / NOPE) + EPS)
            k_ref[:, h * LANE:(h + 1) * LANE] = (jnp.where(m_n, xk * rk * kg, 0.0) + kr).astype(k_ref.dtype)
        v_ref[...] = kvp[:, NH * LANE:].astype(v_ref.dtype)

    return _pcall(body, name=name, grid=(T // tm,),
                  in_specs=[_row_spec(tm, LAT_W), whole(w_q), whole(w_kv), whole(qa_gain), whole(kva_gain), slab, slab, vec, vec],
                  out_specs=[_row_spec(tm, QL), _row_spec(tm, KVL), _row_spec(tm, NH * LANE), _row_spec(tm, NH * LANE + NH * NOPE),
                             _row_spec(tm, NH * LANE), _row_spec(tm, NH * LANE), _row_spec(tm, NH * NOPE)],
                  out_shape=[_sds((T, QL), MX), _sds((T, KVL), MX), _sds((T, NH * LANE), F32), _sds((T, NH * LANE + NH * NOPE), F32),
                             _sds((T, NH * LANE), MX), _sds((T, NH * LANE), MX), _sds((T, NH * NOPE), MX)],
                  )(z, w_q, w_kv, qa_gain, kva_gain, cos, sin, q_gain, k_gain)


def _mla_in_bwd(dq, dk, dv, qp, kvp, z, w_q, w_kv, qa_gain, kva_gain, cos, sin, q_gain, k_gain, *, nb, name, tm=256):
    T = dq.shape[0]
    tps = (T // nb) // tm
    slab = pl.BlockSpec((tm, LANE), lambda i: (i, 0))
    vec = pl.BlockSpec((1, LANE), lambda i: (0, 0))
    whole = lambda a: pl.BlockSpec(a.shape, lambda i: (0, 0))

    def latent_bwd(x, dy, gain):
        r = _rsq(x)
        xhat = x * r
        dxhat = dy * gain
        return r * (dxhat - xhat * jnp.mean(dxhat * xhat, axis=-1, keepdims=True)), jnp.sum(dy * xhat, axis=0, keepdims=True)

    def body(dq_ref, dk_ref, dv_ref, qp_ref, kvp_ref, z_ref, wq_ref, wkv_ref, qa_ref, kva_ref, cos_ref, sin_ref, qg_ref, kg_ref,
             dqp_ref, dkvp_ref, dkr_ref, dql_ref, dkvl_ref, st_ref, sq_ref, sk_ref):
        i = pl.program_id(0)
        qp, kvp = qp_ref, kvp_ref
        m_n, m_r, first_half = _lane_masks((tm, LANE))
        c = cos_ref[...]
        s = sin_ref[...]
        qg = qg_ref[...]
        kg = kg_ref[...]
        dqg = jnp.zeros((1, LANE), F32)
        dkg = jnp.zeros((1, LANE), F32)
        dkr_sum = jnp.zeros((tm, LANE), F32)
        for h in range(NH):
            x = qp[:, h * LANE:(h + 1) * LANE]
            x2 = x * x
            rn = lax.rsqrt(jnp.sum(jnp.where(m_n, x2, 0.0), axis=-1, keepdims=True) * (1.0 / NOPE) + EPS)
            rq = lax.rsqrt(jnp.sum(jnp.where(m_r, x2, 0.0), axis=-1, keepdims=True) * (1.0 / ROPE) + EPS)
            rfac = jnp.where(m_n, rn, jnp.where(m_r, rq, 0.0))
            xhat = x * rfac
            do = dq_ref[:, h * LANE:(h + 1) * LANE] * ATTN_SCALE
            dy = do * c + _rot_t(do * s, first_half, m_r)
            dqg = dqg + jnp.sum(dy * xhat, axis=0, keepdims=True)
            dxhat = dy * qg
            t = dxhat * xhat
            mean_n = jnp.sum(jnp.where(m_n, t, 0.0), axis=-1, keepdims=True) * (1.0 / NOPE)
            mean_r = jnp.sum(jnp.where(m_r, t, 0.0), axis=-1, keepdims=True) * (1.0 / ROPE)
            dqp_ref[:, h * LANE:(h + 1) * LANE] = (
                rfac * (dxhat - xhat * jnp.where(m_n, mean_n, jnp.where(m_r, mean_r, 0.0)))).astype(dqp_ref.dtype)

            xk = kvp[:, h * LANE:(h + 1) * LANE]
            rk = lax.rsqrt(jnp.sum(jnp.where(m_n, xk * xk, 0.0), axis=-1, keepdims=True) * (1.0 / NOPE) + EPS)
            khat = jnp.where(m_n, xk * rk, 0.0)
            dko = dk_ref[:, h * LANE:(h + 1) * LANE]
            dkn = jnp.where(m_n, dko, 0.0)
            dkg = dkg + jnp.sum(dkn * khat, axis=0, keepdims=True)
            dkhat = dkn * kg
            mean_k = jnp.sum(dkhat * khat, axis=-1, keepdims=True) * (1.0 / NOPE)
            dkvp_ref[:, h * LANE:(h + 1) * LANE] = jnp.where(m_n, rk * (dkhat - khat * mean_k), 0.0).astype(dkvp_ref.dtype)
            dkr_sum = dkr_sum + jnp.where(m_r, dko, 0.0)
        dkvp_ref[:, NH * LANE:] = dv_ref[...].astype(dkvp_ref.dtype)

        xr = z_ref[:, Z_KR:Z_KR + LANE].astype(F32)
        rr = lax.rsqrt(jnp.sum(xr * xr, axis=-1, keepdims=True) * (1.0 / ROPE) + EPS)
        rhat = xr * rr
        dyr = dkr_sum * c + _rot_t(dkr_sum * s, first_half, m_r)
        dkg = dkg + jnp.sum(dyr * rhat, axis=0, keepdims=True)
        drhat = dyr * kg
        mean_kr = jnp.sum(drhat * rhat, axis=-1, keepdims=True) * (1.0 / ROPE)
        dkr_ref[...] = jnp.where(m_r, rr * (drhat - rhat * mean_kr), 0.0).astype(dkr_ref.dtype)

        nt = (((1,), (1,)), ((), ()))
        dqn = lax.dot_general(dqp_ref[...], wq_ref[...], nt, preferred_element_type=F32)
        dkvn = lax.dot_general(dkvp_ref[...], wkv_ref[...], nt, preferred_element_type=F32)
        dql, dqa = latent_bwd(z_ref[:, Z_QL:Z_QL + QL].astype(F32), dqn, qa_ref[...])
        dkvl, dkva = latent_bwd(z_ref[:, Z_KV:Z_KV + KVL].astype(F32), dkvn, kva_ref[...])
        dql_ref[...] = dql.astype(dql_ref.dtype)
        dkvl_ref[...] = dkvl.astype(dkvl_ref.dtype)

        @pl.when(i % tps == 0)
        def _():
            st_ref[...] = jnp.zeros_like(st_ref)
            sq_ref[...] = jnp.zeros_like(sq_ref)
            sk_ref[...] = jnp.zeros_like(sk_ref)

        st_ref[0, 0:1, :] += dqg
        st_ref[0, 1:2, :] += dkg
        sq_ref[0, 0:1, :] += dqa
        sk_ref[0, 0:1, :] += dkva

    stat = lambda w: pl.BlockSpec((1, SUBLANE, w), lambda i: (i // tps, 0, 0))
    return _pcall(body, name=name, grid=(T // tm,),
                  in_specs=[_row_spec(tm, NH * LANE), _row_spec(tm, NH * LANE), _row_spec(tm, NH * NOPE),
                            _row_spec(tm, NH * LANE), _row_spec(tm, NH * LANE + NH * NOPE), _row_spec(tm, LAT_W),
                            whole(w_q), whole(w_kv), whole(qa_gain), whole(kva_gain), slab, slab, vec, vec],
                  out_specs=[_row_spec(tm, NH * LANE), _row_spec(tm, NH * LANE + NH * NOPE), slab, _row_spec(tm, QL),
                             _row_spec(tm, KVL), stat(LANE), stat(QL), stat(KVL)],
                  out_shape=[_sds((T, NH * LANE), MX), _sds((T, NH * LANE + NH * NOPE), MX), _sds((T, LANE), MX),
                             _sds((T, QL), MX), _sds((T, KVL), MX), _sds((nb, SUBLANE, LANE), F32),
                             _sds((nb, SUBLANE, QL), F32), _sds((nb, SUBLANE, KVL), F32)],
                  )(dq, dk, dv, qp, kvp, z, w_q, w_kv, qa_gain, kva_gain, cos, sin, q_gain, k_gain)


def _lower_triangle(t):
    return lax.broadcasted_iota(jnp.int32, (t, t), 1) <= lax.broadcasted_iota(jnp.int32, (t, t), 0)


def _attn_fwd(q, k, v, *, nb, name, tq=512, comm=None):
    T = q.shape[0]
    S = T // nb
    tq = _pick(S, tq)
    nq = S // tq
    tk = tq
    npair = NH // 2

    def body(q_ref, k_ref, v_ref, o_ref, lse_ref):
        qi = pl.program_id(2)
        lane = lax.broadcasted_iota(jnp.int32, (tq, LANE), 1)
        qs = [q_ref[:, hh * LANE:(hh + 1) * LANE] for hh in range(2)]

        def block(j, carry, diagonal):
            k0 = pl.multiple_of(j * tk, tk)
            vb = v_ref[pl.ds(k0, tk), :]
            new = []
            for hh in range(2):
                m, l, acc = carry[hh]
                kb = k_ref[pl.ds(k0, tk), hh * LANE:(hh + 1) * LANE]
                s = lax.dot_general(qs[hh], kb, (((1,), (1,)), ((), ())), preferred_element_type=F32)
                if diagonal:
                    s = jnp.where(_lower_triangle(tq), s, NEG)
                m_new = jnp.maximum(m, jnp.max(s, axis=-1, keepdims=True))
                p = jnp.exp(s - m_new)
                alpha = jnp.exp(m - m_new)
                l = alpha * l + jnp.sum(p, axis=-1, keepdims=True)
                acc = alpha * acc + jnp.dot(p.astype(MX), vb, preferred_element_type=F32)
                new.append((m_new, l, acc))
            return tuple(new)

        init = tuple((jnp.full((tq, 1), NEG, F32), jnp.zeros((tq, 1), F32), jnp.zeros((tq, LANE), F32)) for _ in range(2))
        carry = lax.fori_loop(0, qi, lambda j, c: block(j, c, False), init)
        (m0, l0, acc0), (m1, l1, acc1) = block(qi, carry, True)
        o_ref[...] = jnp.where(lane < NOPE, acc0 / l0, acc1 / l1).astype(o_ref.dtype)
        lse_ref[...] = jnp.where(lane < NOPE, m0 + jnp.log(l0), m1 + jnp.log(l1))

    return _pcall(body, name=name, grid=(nb, npair, nq),
                  in_specs=[pl.BlockSpec((tq, 2 * LANE), lambda b, p, i: (b * nq + i, p)),
                            pl.BlockSpec((S, 2 * LANE), lambda b, p, i: (b, p)),
                            pl.BlockSpec((S, LANE), lambda b, p, i: (b, p))],
                  out_specs=[pl.BlockSpec((tq, LANE), lambda b, p, i: (b * nq + i, p)),
                             pl.BlockSpec((tq, LANE), lambda b, p, i: (b * nq + i, p))],
                  out_shape=[_sds((T, NH * NOPE), MX), _sds((T, NH * NOPE), F32)], comm=comm)(q, k, v)


def _attn_bwd(q, k, v, o, lse, do, *, nb, name, tq=512, comm=None):
    T = q.shape[0]
    S = T // nb
    tq = _pick(S, tq)
    nq = S // tq
    tk = tq
    npair = NH // 2

    def body(q_ref, k_ref, v_ref, o_ref, lse_ref, do_ref, dq_ref, dk_ref, dv_ref, delta_ref):
        lane = lax.broadcasted_iota(jnp.int32, (tq, LANE), 1)
        first = lane < NOPE
        dq_ref[...] = jnp.zeros_like(dq_ref)

        def delta_step(qi, _):
            q0 = pl.multiple_of(qi * tq, tq)
            prod = do_ref[pl.ds(q0, tq), :] * o_ref[pl.ds(q0, tq), :].astype(F32)
            d0 = jnp.sum(jnp.where(first, prod, 0.0), axis=-1, keepdims=True)
            d1 = jnp.sum(jnp.where(first, 0.0, prod), axis=-1, keepdims=True)
            delta_ref[pl.ds(q0, tq), :] = jnp.where(first, d0, d1)
            return 0

        lax.fori_loop(0, nq, delta_step, 0)

        def kv_step(kj, _):
            k0 = pl.multiple_of(kj * tk, tk)
            kbs = [k_ref[pl.ds(k0, tk), hh * LANE:(hh + 1) * LANE] for hh in range(2)]
            vb = v_ref[pl.ds(k0, tk), :]

            def q_block(qi, carry, diagonal):
                dk0, dk1, dv = carry
                dks = [dk0, dk1]
                q0 = pl.multiple_of(qi * tq, tq)
                dov = do_ref[pl.ds(q0, tq), :]
                lse_v = lse_ref[pl.ds(q0, tq), :]
                delta_v = delta_ref[pl.ds(q0, tq), :]
                for hh in range(2):
                    qh = q_ref[pl.ds(q0, tq), hh * LANE:(hh + 1) * LANE]
                    dob = jnp.where(first if hh == 0 else jnp.logical_not(first), dov, 0.0).astype(MX)
                    s = lax.dot_general(qh, kbs[hh], (((1,), (1,)), ((), ())), preferred_element_type=F32)
                    p = jnp.exp(s - lse_v[:, hh * NOPE:hh * NOPE + 1])
                    if diagonal:
                        p = jnp.where(_lower_triangle(tq), p, 0.0)
                    dp = lax.dot_general(dob, vb, (((1,), (1,)), ((), ())), preferred_element_type=F32)
                    ds = (p * (dp - delta_v[:, hh * NOPE:hh * NOPE + 1])).astype(MX)
                    dks[hh] = dks[hh] + lax.dot_general(ds, qh, (((0,), (0,)), ((), ())), preferred_element_type=F32)
                    dv = dv + lax.dot_general(p.astype(MX), dob, (((0,), (0,)), ((), ())), preferred_element_type=F32)
                    dq_ref[pl.ds(q0, tq), hh * LANE:(hh + 1) * LANE] += jnp.dot(ds, kbs[hh], preferred_element_type=F32)
                return dks[0], dks[1], dv

            zero = jnp.zeros((tk, LANE), F32)
            carry = q_block(kj, (zero, zero, zero), True)
            dk0, dk1, dv = lax.fori_loop(kj + 1, nq, lambda qi, c: q_block(qi, c, False), carry)
            dk_ref[pl.ds(k0, tk), 0:LANE] = dk0
            dk_ref[pl.ds(k0, tk), LANE:2 * LANE] = dk1
            dv_ref[pl.ds(k0, tk), :] = dv
            return 0

        lax.fori_loop(0, nq, kv_step, 0)

    pair256 = pl.BlockSpec((S, 2 * LANE), lambda b, p: (b, p))
    pair128 = pl.BlockSpec((S, LANE), lambda b, p: (b, p))
    return _pcall(body, name=name, grid=(nb, npair),
                  in_specs=[pair256, pair256, pair128, pair128, pair128, pair128],
                  out_specs=[pair256, pair256, pair128],
                  out_shape=[_sds((T, NH * LANE), F32), _sds((T, NH * LANE), F32), _sds((T, NH * NOPE), F32)],
                  scratch=[pltpu.VMEM((S, LANE), F32)], comm=comm)(q, k, v, o, lse, do)


def _run(plan, fn, *args, name, **kw):
    rider = plan.rider(name)
    if rider is None:
        return fn(*args, name=name, **kw)
    outs, landed = fn(*args, name=name, comm=rider, **kw)
    plan.landed(name, landed)
    return outs


def _layer_fwd_bwd(x, tgt, mod3, cos, sin, plan, P):
    nb = mod3.shape[0]
    W = plan.W
    h1, gu1, a1 = _run(plan, _ffn_up_first, x, mod3, P["norm_ffn1"], W["ffn1_in"], sub=0, name="ffn1_up_a")
    gu1, a1 = _run(plan, _ffn_up_second, h1, W["ffn1_in"], gu1, a1, name="ffn1_up_b")
    f1, x1, h2 = _run(plan, _mm_resid_norm, a1, W["ffn1_out"], x, mod3, P["norm_mix"], sub=1, coef=0.5, name="ffn1_out")
    z = _run(plan, _mm, h2, W["w_in"], mode="nt", out_dtype=MX, name="mix_in", tn=1664)
    pooled, mixed, scaled = _pool_fwd(z, W["pool_grp"], P["pool_scale"], nb=nb, name="pool_fwd")
    br_pool = _mm(scaled, W["pool_proj"], mode="nn", out_dtype=MX, name="pool_proj")
    qn, kvn, qp, kvp, q, k, v = _mla_in_fwd(z, W["q_up"], W["kv_up"], P["q_a_norm"], P["kv_a_norm"], cos, sin,
                                            P["q_gain"], P["k_gain"], name="mla_in_fwd")
    attn, lse = _run(plan, _attn_fwd, q, k, v, nb=nb, name="attn_fwd")
    br_mla, merged, mo, x2, h3 = _mix_out_fwd(z, br_pool, attn, W["mla_proj"], W["w_out"], x1, mod3, P["norm_ffn2"], sub=2,
                                              name="mix_out", tm=256)
    gu2, a2 = _ffn_up(h3, W["ffn2_in"], name="ffn2_up")
    dy, df2, st_fin, loss = _mm_loss(a2, W["ffn2_out"], x2, tgt, mod3, name="ffn2_out_loss")

    plan.grad("ffn2_out", _dw(a2, df2, name="d_ffn2_out"))
    dgu2 = _ffn_dact(df2, gu2, W["ffn2_out"], name="ffn2_dact")
    plan.grad("ffn2_in", _ffn_dw_in(h3, dgu2, name="d_ffn2_in"))
    dx2, dmo, st3 = _run(plan, _ffn_dh_norm, dgu2, W["ffn2_in"], x2, dy, mo, mod3, P["norm_ffn2"], sub=2, coef=1.0,
                         name="d_ffn2_h")
    plan.grad("w_out", _dw(merged, dmo, name="d_mix_out"))
    dbr_pool, dbr_mla, dgates, dattn = _mix_out_bwd(dmo, W["w_out"], z, br_pool, br_mla, W["mla_proj"], name="mix_out_bwd", tm=256)
    plan.grad("pool_proj", _dw(scaled, dbr_pool, name="d_pool_proj"))
    dscaled = _mm(dbr_pool, W["pool_proj"], mode="nt", out_dtype=F32, name="d_pool_scaled")
    du_pool, d_pool_grp, d_pool_scale = _pool_bwd(dscaled, mixed, pooled, W["pool_grp"], P["pool_scale"], nb=nb, name="pool_bwd")
    plan.grad("mla_proj", _dw(attn, dbr_mla, name="d_mla_proj"))
    dq, dk, dv = _run(plan, _attn_bwd, q, k, v, attn, lse, dattn, nb=nb, name="attn_bwd")
    dqp, dkvp, dkr, dql, dkvl, st_prep, st_q, st_kv = _mla_in_bwd(
        dq, dk, dv, qp, kvp, z, W["q_up"], W["kv_up"], P["q_a_norm"], P["kv_a_norm"], cos, sin, P["q_gain"], P["k_gain"],
        nb=nb, name="mla_in_bwd")
    plan.grad("q_up", _dw(qn, dqp, name="d_q_up"))
    plan.grad("kv_up", _dw(kvn, dkvp, name="d_kv_up"))
    dz = jnp.concatenate([du_pool, dql, dkvl, dkr, dgates], axis=1)
    plan.grad("w_in", _run(plan, _dw, h2, dz, name="d_mix_in", tm=256, tn=1664, out_t=True))
    dx1, df1, st2 = _run(plan, _mm_norm_bwd, dz, W["w_in"], x1, dx2, f1, mod3, P["norm_mix"], sub=1, coef=0.5, name="d_mix_h")
    plan.grad("ffn1_out", _run(plan, _dw, a1, df1, name="d_ffn1_out"))
    dgu1 = _run(plan, _ffn_dact, df1, gu1, W["ffn1_out"], name="ffn1_dact")
    half = x.shape[1] // 2
    plan.grad("ffn1_in@0", _run(plan, _ffn_dw_in, h1, dgu1, rows=(0, half), name="d_ffn1_in_a"))
    plan.grad("ffn1_in@1", _run(plan, _ffn_dw_in, h1, dgu1, rows=(half, half), name="d_ffn1_in_b"))
    dh1 = _run(plan, _ffn_dh, dgu1, W["ffn1_in"], name="d_ffn1_h")
    grad_x, st1 = _run(plan, _bwd_block, x, dh1, dx1, None, mod3, P["norm_ffn1"], sub=0, coef=0.0, name="bwd_norm1")

    return loss, grad_x, (st1, st2, st3, st_fin, st_q, st_kv, st_prep, d_pool_scale), d_pool_grp


def _padded(rows):
    return rows + -rows % (4 * SUBLANE)


def _pad_rows(a):
    pad = [(0, 0)] * a.ndim
    pad[-2] = (0, _padded(a.shape[-2]) - a.shape[-2])
    return jnp.pad(a, pad)


def _w_in_pieces(shard):
    runs = []
    for ref0, ref1, k0 in ((0, 1152, 0), (1152, 1184, Z_KR + KR_LANE), (1184, N_CHIP * shard, Z_GP)):
        for q in range(N_CHIP):
            lo, hi = max(ref0, q * shard), min(ref1, (q + 1) * shard)
            if lo < hi:
                runs.append((k0 + lo - ref0, q * _padded(shard) + lo - q * shard, hi - lo))
    return runs


def _w_in_stacked_to_kernel(st, shard):
    flat = st.reshape(-1, st.shape[2])
    parts, at = [], 0
    for k0, s0, n in sorted(_w_in_pieces(shard)):
        parts += [jnp.zeros((k0 - at, flat.shape[1]), st.dtype)] * (k0 > at) + [flat[s0:s0 + n]]
        at = k0 + n
    assert at == Z_W
    return jnp.concatenate(parts, axis=0)


def _w_in_kernel_to_stacked(gt, shard):
    parts, at = [], 0
    for s0, k0, n in sorted((s, k, n) for k, s, n in _w_in_pieces(shard)) + [(N_CHIP * _padded(shard), 0, 0)]:
        parts += [jnp.zeros((s0 - at, gt.shape[1]), gt.dtype)] * (s0 > at) + [gt[k0:k0 + n]] * (n > 0)
        at = s0 + n
    return jnp.concatenate(parts, axis=0).reshape(N_CHIP, _padded(shard), gt.shape[1])


def _q_up_to_kernel(w):
    k = w.shape[0]
    return jnp.pad(w.reshape(k, NH, NOPE + ROPE), ((0, 0), (0, 0), (0, LANE - NOPE - ROPE))).reshape(k, NH * LANE)


def _q_up_from_kernel(g):
    k = g.shape[0]
    return g.reshape(k, NH, LANE)[:, :, :NOPE + ROPE].reshape(k, NH * (NOPE + ROPE))


def _kv_up_to_kernel(w):
    k = w.shape[0]
    w3 = w.reshape(k, NH, 2 * NOPE)
    kpart = jnp.pad(w3[:, :, :NOPE], ((0, 0), (0, 0), (0, LANE - NOPE))).reshape(k, NH * LANE)
    return jnp.concatenate([kpart, w3[:, :, NOPE:].reshape(k, NH * NOPE)], axis=1)


def _kv_up_from_kernel(g):
    k = g.shape[0]
    kpart = g[:, :NH * LANE].reshape(k, NH, LANE)[:, :, :NOPE]
    vpart = g[:, NH * LANE:].reshape(k, NH, NOPE)
    return jnp.concatenate([kpart, vpart], axis=2).reshape(k, NH * 2 * NOPE)


def _gain_slab(nope, rope):
    return jnp.concatenate([nope, rope, jnp.zeros((1, LANE - NOPE - ROPE), nope.dtype)], axis=1)


def _rope_tables(positions):
    inv_freq = 10000.0 ** (-jnp.arange(0, ROPE, 2, dtype=F32) / ROPE)
    ang = positions.astype(F32)[:, None] * inv_freq
    ang = jnp.concatenate([ang, ang], axis=-1)
    t = positions.shape[0]
    cos = jnp.concatenate([jnp.ones((t, KR_LANE), F32), jnp.cos(ang), jnp.ones((t, LANE - KR_LANE - ROPE), F32)], axis=1)
    sin = jnp.concatenate([jnp.zeros((t, KR_LANE), F32), jnp.sin(ang), jnp.zeros((t, LANE - KR_LANE - ROPE), F32)], axis=1)
    return cos, sin


def _coords():
    return lax.axis_index("x"), lax.axis_index("y"), lax.axis_index("c")


HBM_SPEC = pl.BlockSpec(memory_space=pl.ANY)
VMEM_SPEC = pl.BlockSpec(memory_space=pltpu.VMEM)


CHIP_RELS = ((1, 0), (0, 1), (1, 1))


def _gather_comm(shards):
    n = len(shards)

    def ici(ins, outs, sems, a, j, x, y, cc):
        half = ins[a].shape[0] // 2
        mine = pl.ds(cc * half, half)
        dx, dy = CHIP_RELS[j]
        return pltpu.make_async_remote_copy(src_ref=ins[a].at[mine], dst_ref=outs[a].at[2 * x + y, mine],
                                            send_sem=sems[0].at[a, j], recv_sem=sems[1].at[a, j],
                                            device_id=(x ^ dx, y ^ dy, cc), device_id_type=MESH)

    def d2d(ins, outs, sems, a, j, x, y, cc, half_of):
        half = ins[a].shape[0] // 2
        dx, dy = CHIP_RELS[j]
        landed = outs[a].at[2 * (x ^ dx) + (y ^ dy), pl.ds(half_of * half, half)]
        return pltpu.make_async_remote_copy(src_ref=landed, dst_ref=landed, send_sem=sems[2].at[a, j], recv_sem=sems[3].at[a, j],
                                            device_id=(x, y, 1 - cc), device_id_type=MESH)

    def own(ins, outs, sems, a, x, y):
        return pltpu.make_async_copy(ins[a], outs[a].at[2 * x + y], sems[4].at[a])

    def start(ins, outs, sems):
        x, y, cc = _coords()
        for a in range(n):
            own(ins, outs, sems, a, x, y).start()
            for j in range(3):
                ici(ins, outs, sems, a, j, x, y, cc).start()

    def finish(ins, outs, sems):
        x, y, cc = _coords()
        for a in range(n):
            for j in range(3):
                ici(ins, outs, sems, a, j, x, y, cc).wait_recv()
                d2d(ins, outs, sems, a, j, x, y, cc, cc).start()
        for a in range(n):
            for j in range(3):
                d2d(ins, outs, sems, a, j, x, y, cc, 1 - cc).wait_recv()
        for a in range(n):
            for j in range(3):
                ici(ins, outs, sems, a, j, x, y, cc).wait_send()
                d2d(ins, outs, sems, a, j, x, y, cc, cc).wait_send()
            own(ins, outs, sems, a, x, y).wait()

    dma = pltpu.SemaphoreType.DMA
    return _Comm(shards, [_sds((N_CHIP,) + s.shape, s.dtype) for s in shards],
                 [dma((n, 3)), dma((n, 3)), dma((n, 3)), dma((n, 3)), dma((n,))], start, finish)


def _swap_comm(parts):
    n = len(parts)

    def copy(ins, outs, sems, a):
        x, y, cc = _coords()
        half = ins[a].shape[1] // 2
        return pltpu.make_async_remote_copy(src_ref=ins[a].at[:, pl.ds((1 - cc) * half, half)], dst_ref=outs[a],
                                            send_sem=sems[0].at[a], recv_sem=sems[1].at[a], device_id=(x, y, 1 - cc),
                                            device_id_type=MESH)

    def start(ins, outs, sems):
        for a in range(n):
            copy(ins, outs, sems, a).start()

    def finish(ins, outs, sems):
        for a in range(n):
            copy(ins, outs, sems, a).wait()

    dma = pltpu.SemaphoreType.DMA
    return _Comm(parts, [_sds((p.shape[0], p.shape[1] // 2, p.shape[2]), p.dtype) for p in parts], [dma((n,)), dma((n,))],
                 start, finish)


def _add_half(full, other, cidx, *, name):
    nch, r, c = full.shape
    half = r // 2
    tr = _pick_rows(half)
    nbk = half // tr
    grid_spec = pltpu.PrefetchScalarGridSpec(
        num_scalar_prefetch=1, grid=(nch, nbk),
        in_specs=[pl.BlockSpec((1, tr, c), lambda j, i, cref: (j, cref[0] * nbk + i, 0)),
                  pl.BlockSpec((1, tr, c), lambda j, i, cref: (j, i, 0))],
        out_specs=pl.BlockSpec((1, tr, c), lambda j, i, cref: (j, i, 0)))

    def body(cref, a_ref, b_ref, o_ref):
        o_ref[...] = (a_ref[...] + b_ref[...]).astype(o_ref.dtype)

    return _pcall(body, name=name, out_shape=_sds((nch, half, c), MX), grid_spec=grid_spec)(cidx, full, other)


def _pick_rows(rows, target=512):
    best = None
    for t in range(16, min(rows, target) + 1, 16):
        if rows % t == 0:
            best = t
    return rows if best is None else best


def _exchange_comm(parts):
    n = len(parts)

    def send(ins, outs, sems, a, j, x, y, cc):
        dx, dy = CHIP_RELS[j]
        return pltpu.make_async_remote_copy(src_ref=ins[a].at[2 * (x ^ dx) + (y ^ dy)], dst_ref=outs[a].at[2 * x + y],
                                            send_sem=sems[0].at[a, j], recv_sem=sems[1].at[a, j],
                                            device_id=(x ^ dx, y ^ dy, cc), device_id_type=MESH)

    def landing(ins, outs, sems, a, j, x, y, cc):
        dx, dy = CHIP_RELS[j]
        peer_chip = 2 * (x ^ dx) + (y ^ dy)
        return pltpu.make_async_remote_copy(src_ref=ins[a].at[peer_chip], dst_ref=outs[a].at[peer_chip], send_sem=sems[0].at[a, j],
                                            recv_sem=sems[1].at[a, j], device_id=(x, y, cc), device_id_type=MESH)

    def own(ins, outs, sems, a, x, y):
        return pltpu.make_async_copy(ins[a].at[2 * x + y], outs[a].at[2 * x + y], sems[2].at[a])

    def start(ins, outs, sems):
        x, y, cc = _coords()
        for a in range(n):
            own(ins, outs, sems, a, x, y).start()
            for j in range(3):
                send(ins, outs, sems, a, j, x, y, cc).start()

    def finish(ins, outs, sems):
        x, y, cc = _coords()
        for a in range(n):
            for j in range(3):
                landing(ins, outs, sems, a, j, x, y, cc).wait_recv()
        for a in range(n):
            for j in range(3):
                send(ins, outs, sems, a, j, x, y, cc).wait_send()
            own(ins, outs, sems, a, x, y).wait()

    dma = pltpu.SemaphoreType.DMA
    return _Comm(parts, [_sds(p.shape, p.dtype) for p in parts], [dma((n, 3)), dma((n, 3)), dma((n,))], start, finish)


def _sum_chips(q, cidx, *, name):
    nch, h, c = q.shape
    tr = _pick_rows(h)
    nbk = h // tr
    grid_spec = pltpu.PrefetchScalarGridSpec(
        num_scalar_prefetch=1, grid=(nbk,),
        in_specs=[pl.BlockSpec((nch, tr, c), lambda i, cref: (0, i, 0))],
        out_specs=pl.BlockSpec((tr, c), lambda i, cref: (cref[0] * nbk + i, 0)))

    def body(cref, q_ref, o_ref):
        acc = q_ref[0].astype(F32) + q_ref[1].astype(F32)
        acc = acc + q_ref[2].astype(F32)
        o_ref[...] = acc + q_ref[3].astype(F32)

    return _pcall(body, name=name, out_shape=_sds((2 * h, c), F32), grid_spec=grid_spec)(cidx, q)


def _join_comm(fulls):
    n = len(fulls)

    def half_copy(outs, sems, a, which):
        x, y, cc = _coords()
        h = outs[a].shape[0] // 2
        rows = outs[a].at[pl.ds((cc if which == 0 else 1 - cc) * h, h)]
        return pltpu.make_async_remote_copy(src_ref=rows, dst_ref=rows, send_sem=sems[0].at[a], recv_sem=sems[1].at[a],
                                            device_id=(x, y, 1 - cc), device_id_type=MESH)

    def start(ins, outs, sems):
        for a in range(n):
            half_copy(outs, sems, a, 0).start()

    def finish(ins, outs, sems):
        for a in range(n):
            half_copy(outs, sems, a, 1).wait_recv()
        for a in range(n):
            half_copy(outs, sems, a, 0).wait_send()

    dma = pltpu.SemaphoreType.DMA
    return _Comm(fulls, [_sds(p.shape, p.dtype) for p in fulls], [dma((n,)), dma((n,))], start, finish,
                 aliases={a: a for a in range(n)})


def _ada_prologue(c, w, b, *, name, comm=None):
    nb, dm = c.shape
    n = w.shape[1]

    def body(c_ref, w_ref, b_ref, call_ref, land_ref, cpad, mod_s, g_send, g_recv, m_send, m_recv):
        x, y, cc = _coords()
        me = 4 * x + 2 * y + cc
        chip = 2 * x + y
        cpad[...] = jnp.zeros_like(cpad)
        cpad[0:nb, :] = c_ref[...]
        copies = []
        for kk in range(1, N_DEV):
            peer = (x ^ (kk >> 2), y ^ ((kk >> 1) & 1), cc ^ (kk & 1))
            cp = pltpu.make_async_remote_copy(src_ref=cpad, dst_ref=call_ref.at[me], send_sem=g_send.at[kk - 1],
                                              recv_sem=g_recv.at[kk - 1], device_id=peer, device_id_type=MESH)
            cp.start()
            copies.append(cp)
        call_ref[me] = cpad[...]
        for kk in range(1, N_DEV):
            pltpu.make_async_remote_copy(src_ref=cpad, dst_ref=call_ref.at[me ^ kk], send_sem=g_send.at[kk - 1],
                                         recv_sem=g_recv.at[kk - 1], device_id=(x, y, cc), device_id_type=MESH).wait_recv()
        cv = call_ref[...].reshape(N_DEV * SUBLANE, dm)
        act = (cv * _sigmoid(cv)).astype(MX)
        mod = jnp.dot(act, w_ref[...].astype(MX), preferred_element_type=F32) + b_ref[...]
        mod_s[...] = mod.reshape(N_DEV, SUBLANE, n)
        for j, (dx, dy) in enumerate(CHIP_RELS):
            cp = pltpu.make_async_remote_copy(src_ref=mod_s.at[4 * (x ^ dx) + 2 * (y ^ dy) + cc], dst_ref=land_ref.at[chip],
                                              send_sem=m_send.at[j], recv_sem=m_recv.at[j],
                                              device_id=(x ^ dx, y ^ dy, cc), device_id_type=MESH)
            cp.start()
            copies.append(cp)
        land_ref[chip] = mod_s[me]
        for j, (dx, dy) in enumerate(CHIP_RELS):
            pltpu.make_async_remote_copy(src_ref=mod_s.at[me], dst_ref=land_ref.at[2 * (x ^ dx) + (y ^ dy)], send_sem=m_send.at[j],
                                         recv_sem=m_recv.at[j], device_id=(x, y, cc), device_id_type=MESH).wait_recv()
        for cp in copies:
            cp.wait_send()

    dma = pltpu.SemaphoreType.DMA
    return _pcall(body, name=name, in_specs=[VMEM_SPEC] * 3, out_specs=[VMEM_SPEC] * 2,
                  out_shape=[_sds((N_DEV, SUBLANE, dm), F32), _sds((N_CHIP, SUBLANE, n), F32)],
                  scratch=[pltpu.VMEM((SUBLANE, dm), F32), pltpu.VMEM((N_DEV, SUBLANE, n), F32),
                           dma((N_DEV - 1,)), dma((N_DEV - 1,)), dma((3,)), dma((3,))], comm=comm)(c, w, b)


def _ada_bwd(c_all, dmod_cols, *, name):
    m, kdim = c_all.shape
    n = dmod_cols.shape[1]
    tn = _pick(n, 1152)

    def body(c_ref, d_ref, o_ref):
        cv = c_ref[...]
        act = (cv * _sigmoid(cv)).astype(MX)
        o_ref[...] = lax.dot_general(act, d_ref[...].astype(MX), (((0,), (0,)), ((), ())), preferred_element_type=F32)

    return _pcall(body, name=name, out_shape=_sds((kdim, n), F32), grid=(n // tn,),
                  in_specs=[pl.BlockSpec((m, kdim), lambda j: (0, 0)), pl.BlockSpec((m, tn), lambda j: (0, j))],
                  out_specs=pl.BlockSpec((kdim, tn), lambda j: (0, j)))(c_all, dmod_cols)


SLAB_W = 1024
RED_ROWS = 8


LOSS_LANE = SLAB_W - LANE


def _pack_stats(st1, st2, st3, st_fin, st_q, st_kv, st_prep, d_pool_scale, loss8, *, name):
    nb = st1.shape[0]
    dm_rows = -(-9 * nb // SUBLANE) * SUBLANE

    def body(s1, s2, s3, sf, sq, skv, sp, sps, loss_ref, o_ref):
        o_ref[...] = jnp.zeros_like(o_ref)
        for s in range(nb):
            rows = [s1[s, 0:1, :], s1[s, 1:2, :], s2[s, 3:4, :], s2[s, 0:1, :], s2[s, 1:2, :], s3[s, 3:4, :], s3[s, 0:1, :],
                    s3[s, 1:2, :], sf[s, 0:1, :]]
            for k, row in enumerate(rows):
                o_ref[9 * s + k:9 * s + k + 1, :] = row

        def over_seq(ref, r):
            acc = ref[0, r:r + 1, :]
            for s in range(1, nb):
                acc = acc + ref[s, r:r + 1, :]
            return acc

        o_ref[dm_rows + 0:dm_rows + 1, :] = over_seq(s1, 2)
        o_ref[dm_rows + 1:dm_rows + 2, :] = over_seq(s2, 2)
        o_ref[dm_rows + 2:dm_rows + 3, :] = over_seq(s3, 2)
        o_ref[dm_rows + 3:dm_rows + 4, 0:POOL_W] = over_seq(sps, 0)
        o_ref[dm_rows + 4:dm_rows + 5, 0:QL] = over_seq(sq, 0)
        o_ref[dm_rows + 5:dm_rows + 6, 0:KVL] = over_seq(skv, 0)
        o_ref[dm_rows + 6:dm_rows + 7, 0:LANE] = over_seq(sp, 0)
        o_ref[dm_rows + 7:dm_rows + 8, 0:LANE] = over_seq(sp, 1)
        o_ref[dm_rows + 7:dm_rows + 8, LOSS_LANE:] = loss_ref[0:1, :]

    return _pcall(body, name=name, out_shape=_sds((dm_rows + RED_ROWS, SLAB_W), F32), in_specs=[VMEM_SPEC] * 9,
                  out_specs=VMEM_SPEC)(st1, st2, st3, st_fin, st_q, st_kv, st_prep, d_pool_scale, loss8)


def _small_allreduce(slab, pool, *, name, comm=None):
    rows, w = slab.shape
    dm = rows - RED_ROWS
    prow, pw = pool.shape
    crow = RED_ROWS + 2 * dm

    def body(slab_ref, pool_ref, red_ref, dmod_ref, ptot_ref, sib_slab, sib_pool, chip_slab, chip_pool, land_slab, land_pool,
             a_send, a_recv, b_send, b_recv):
        x, y, cc = _coords()
        chip = 2 * x + y
        sib = (x, y, 1 - cc)
        to_sib = [pltpu.make_async_remote_copy(src_ref=slab_ref, dst_ref=sib_slab, send_sem=a_send.at[0], recv_sem=a_recv.at[0],
                                               device_id=sib, device_id_type=MESH),
                  pltpu.make_async_remote_copy(src_ref=pool_ref, dst_ref=sib_pool, send_sem=a_send.at[1], recv_sem=a_recv.at[1],
                                               device_id=sib, device_id_type=MESH)]
        for cp in to_sib:
            cp.start()
        for cp in to_sib:
            cp.wait()
        mine_dm, theirs_dm = slab_ref[0:dm, :], sib_slab[0:dm, :]
        chip_slab[0:RED_ROWS, :] = slab_ref[dm:, :] + sib_slab[dm:, :]
        chip_slab[RED_ROWS:RED_ROWS + dm, :] = jnp.where(cc == 0, mine_dm, theirs_dm)
        chip_slab[RED_ROWS + dm:, :] = jnp.where(cc == 0, theirs_dm, mine_dm)
        chip_pool[...] = pool_ref[...] + sib_pool[...]

        sends = []
        for j, (dx, dy) in enumerate(CHIP_RELS):
            peer = (x ^ dx, y ^ dy, cc)
            sends.append(pltpu.make_async_remote_copy(src_ref=chip_slab, dst_ref=land_slab.at[chip], send_sem=b_send.at[j, 0],
                                                      recv_sem=b_recv.at[j, 0], device_id=peer, device_id_type=MESH))
            sends.append(pltpu.make_async_remote_copy(src_ref=chip_pool, dst_ref=land_pool.at[chip], send_sem=b_send.at[j, 1],
                                                      recv_sem=b_recv.at[j, 1], device_id=peer, device_id_type=MESH))
        for cp in sends:
            cp.start()
        land_slab[chip] = chip_slab[...]
        land_pool[chip] = chip_pool[...]
        for j, (dx, dy) in enumerate(CHIP_RELS):
            peer_chip = 2 * (x ^ dx) + (y ^ dy)
            pltpu.make_async_remote_copy(src_ref=chip_slab, dst_ref=land_slab.at[peer_chip], send_sem=b_send.at[j, 0],
                                         recv_sem=b_recv.at[j, 0], device_id=(x, y, cc), device_id_type=MESH).wait_recv()
            pltpu.make_async_remote_copy(src_ref=chip_pool, dst_ref=land_pool.at[peer_chip], send_sem=b_send.at[j, 1],
                                         recv_sem=b_recv.at[j, 1], device_id=(x, y, cc), device_id_type=MESH).wait_recv()
        red = land_slab[0, 0:RED_ROWS, :]
        ptot = land_pool[0]
        for ch in range(1, N_CHIP):
            red = red + land_slab[ch, 0:RED_ROWS, :]
            ptot = ptot + land_pool[ch]
        red_ref[...] = red
        ptot_ref[...] = ptot
        for ch in range(N_CHIP):
            dmod_ref[2 * dm * ch:2 * dm * (ch + 1), :] = land_slab[ch, RED_ROWS:, :]
        for cp in sends:
            cp.wait_send()

    dma = pltpu.SemaphoreType.DMA
    return _pcall(body, name=name, in_specs=[VMEM_SPEC, VMEM_SPEC], out_specs=[VMEM_SPEC] * 3,
                  out_shape=[_sds((RED_ROWS, w), F32), _sds((N_DEV * dm, w), F32), _sds((prow, pw), F32)],
                  scratch=[pltpu.VMEM((rows, w), F32), pltpu.VMEM((prow, pw), F32), pltpu.VMEM((crow, w), F32),
                           pltpu.VMEM((prow, pw), F32), pltpu.VMEM((N_CHIP, crow, w), F32), pltpu.VMEM((N_CHIP, prow, pw), F32),
                           dma((2,)), dma((2,)), dma((3, 2)), dma((3, 2))], comm=comm)(slab, pool)


def _adamw_math(w, g, m, v):
    mn = ADAM_B1 * m + (1.0 - ADAM_B1) * g
    vn = ADAM_B2 * v + (1.0 - ADAM_B2) * (g * g)
    bc1 = 1.0 / (1.0 - ADAM_B1 ** ADAM_STEP)
    bc2 = 1.0 / (1.0 - ADAM_B2 ** ADAM_STEP)
    return -ADAM_LR * ((mn * bc1) / (jnp.sqrt(vn * bc2) + ADAM_EPS) + ADAM_WD * w), mn, vn


def _small_update(red, dmod_all, pool_total, nb, params, *, name):
    names = list(SMALL)
    dm = dmod_all.shape[0] // N_DEV

    def grad_of(nm, red_ref, dmod_ref, ptot_ref):
        if nm == "b_ada":
            acc = None
            for d in range(N_DEV):
                for s in range(nb):
                    blk = dmod_ref[d * dm + 9 * s:d * dm + 9 * s + 9, :]
                    acc = blk if acc is None else acc + blk
            return jnp.concatenate([acc[k:k + 1, :] for k in range(9)], axis=1)
        if nm == "pool_grp":
            return ptot_ref[...]
        row, lo, n = {"norm_ffn1": (0, 0, D), "norm_mix": (1, 0, D), "norm_ffn2": (2, 0, D), "pool_scale": (3, 0, POOL_W),
                      "q_a_norm": (4, 0, QL), "kv_a_norm": (5, 0, KVL), "q_norm_nope": (6, 0, NOPE),
                      "q_norm_rope": (6, NOPE, ROPE), "k_norm_nope": (7, 0, NOPE), "k_norm_rope": (7, KR_LANE, ROPE)}[nm]
        return red_ref[row:row + 1, lo:lo + n]

    def body(*refs):
        red_ref, dmod_ref, ptot_ref = refs[:3]
        ins = refs[3:3 + 3 * len(names)]
        outs = refs[3 + 3 * len(names):]
        outs[4 * len(names)][...] = red_ref[RED_ROWS - 1:RED_ROWS, LOSS_LANE:]
        for i, nm in enumerate(names):
            g = grad_of(nm, red_ref, dmod_ref, ptot_ref)
            d, mn, vn = _adamw_math(ins[3 * i][...], g, ins[3 * i + 1][...], ins[3 * i + 2][...])
            outs[4 * i][...] = g
            outs[4 * i + 1][...] = d
            outs[4 * i + 2][...] = mn
            outs[4 * i + 3][...] = vn

    flat_in = [a for nm in names for a in params[nm]]
    out_shape = [_sds(params[nm][0].shape, F32) for nm in names for _ in range(4)] + [_sds((1, LANE), F32)]
    res = _pcall(body, name=name, in_specs=[VMEM_SPEC] * (3 + len(flat_in)), out_specs=[VMEM_SPEC] * len(out_shape),
                 out_shape=out_shape)(red, dmod_all, pool_total, *flat_in)
    return {nm: tuple(res[4 * i:4 * i + 4]) for i, nm in enumerate(names)}, res[-1]


def _adamw(w, g, m, v, *, name, comm=None):
    r, c = w.shape
    tr = _pick_rows(r, 256)
    tc = c if tr < r else _pick(c, 256)
    spec = pl.BlockSpec((tr, tc), lambda i, j: (i, j))
    bc1 = 1.0 / (1.0 - ADAM_B1 ** ADAM_STEP)
    bc2 = 1.0 / (1.0 - ADAM_B2 ** ADAM_STEP)

    def body(w_ref, g_ref, m_ref, v_ref, d_ref, mo_ref, vo_ref):
        gv = g_ref[...]
        mn = ADAM_B1 * m_ref[...] + (1.0 - ADAM_B1) * gv
        vn = ADAM_B2 * v_ref[...] + (1.0 - ADAM_B2) * (gv * gv)
        mo_ref[...] = mn
        vo_ref[...] = vn
        d_ref[...] = -ADAM_LR * ((mn * bc1) / (jnp.sqrt(vn * bc2) + ADAM_EPS) + ADAM_WD * w_ref[...])

    out = _sds((r, c), F32)
    return _pcall(body, name=name, out_shape=[out, out, out], grid=(r // tr, c // tc), in_specs=[spec] * 4, out_specs=[spec] * 3,
                  comm=comm)(w, g, m, v)


BIG = ("w_ffn1_in", "w_ffn1_out", "w_in", "w_pool_proj", "w_q_up", "w_kv_up", "w_mla_proj", "w_out", "w_ffn2_in", "w_ffn2_out")
ROW_SHARDED = ("w_ffn1_out", "w_out", "w_ffn2_out")
KERNEL_NAME = {"w_ffn1_in": "ffn1_in", "w_ffn1_out": "ffn1_out", "w_in": "w_in", "w_pool_proj": "pool_proj", "w_q_up": "q_up",
               "w_kv_up": "kv_up", "w_mla_proj": "mla_proj", "w_out": "w_out", "w_ffn2_in": "ffn2_in", "w_ffn2_out": "ffn2_out"}
WEIGHTS = ("w_ada", "b_ada", "norm_ffn1", "w_ffn1_in", "w_ffn1_out", "norm_mix", "w_in", "pool_grp", "pool_scale", "w_pool_proj",
           "q_a_norm", "w_q_up", "kv_a_norm", "w_kv_up", "q_norm_nope", "q_norm_rope", "k_norm_nope", "k_norm_rope", "w_mla_proj",
           "w_out", "norm_ffn2", "w_ffn2_in", "w_ffn2_out")
SMALL = ("b_ada", "norm_ffn1", "norm_mix", "pool_grp", "pool_scale", "q_a_norm", "kv_a_norm", "q_norm_nope", "q_norm_rope",
         "k_norm_nope", "k_norm_rope", "norm_ffn2")


def _assemble(name, stacked):
    if name in ROW_SHARDED:
        return stacked.reshape(stacked.shape[0] * stacked.shape[1], stacked.shape[2])
    return jnp.transpose(stacked, (1, 0, 2)).reshape(stacked.shape[1], stacked.shape[0] * stacked.shape[2])


def _split(name, full):
    if name in ROW_SHARDED:
        return full.reshape(N_CHIP, full.shape[0] // N_CHIP, full.shape[1])
    return jnp.transpose(full.reshape(full.shape[0], N_CHIP, full.shape[1] // N_CHIP), (1, 0, 2))


GU = ("w_ffn1_in", "w_ffn2_in")
NARROW = ("w_in", "w_q_up")
TRANSPOSED = ("w_in",)
W_IN_SHARD = (Z_W - LANE + ROPE) // N_CHIP


MIX_SMALL = ("w_out", "w_pool_proj", "w_mla_proj", "w_q_up", "w_kv_up")
RIDES = {
    "ada_prologue": (("gather", ("w_ffn1_in",)),),
    "ffn1_up_a": (("gather", ("w_ffn1_out",)),),
    "ffn1_up_b": (("gather", ("w_in",)),),
    "mix_in": (("gather", MIX_SMALL),),
    "attn_fwd": (("gather", ("w_ffn2_in", "w_ffn2_out")),),
    "d_ffn2_h": (("swap", ("w_ffn2_out", "w_ffn2_in")),),
    "attn_bwd": (("exchange", ("w_ffn2_out", "w_ffn2_in")),),
    "d_mix_in": (("swap", MIX_SMALL),),
    "d_mix_h": (("exchange", MIX_SMALL), ("swap", ("w_in",))),
    "ffn1_dact": (("exchange", ("w_in",)), ("swap", ("w_ffn1_out",))),
    "d_ffn1_in_a": (("exchange", ("w_ffn1_out",)),),
    "d_ffn1_in_b": (("swap", ("w_ffn1_in@0",)),),
    "d_ffn1_h": (("exchange", ("w_ffn1_in@0",)), ("swap", ("w_ffn1_in@1",))),
    "bwd_norm1": (("exchange", ("w_ffn1_in@1",)),),
    "small_allreduce": (("join", tuple(n for n in BIG if n != "w_ffn1_in") + ("w_ffn1_in@0", "w_ffn1_in@1")),),
}


def _kname(n):
    base, _, part = n.partition("@")
    return KERNEL_NAME[base] + ("@" + part if part else "")


def _both(comms):
    if len(comms) == 1:
        return comms[0]
    ins, outs, sems, aliases, spans = [], [], [], {}, []
    for c in comms:
        spans.append((len(ins), len(c.ins), len(outs), len(c.out_shapes), len(sems), len(c.sems)))
        aliases.update({len(ins) + i: len(outs) + o for i, o in c.aliases.items()})
        ins, outs, sems = ins + c.ins, outs + c.out_shapes, sems + c.sems

    def each(which):
        def run(i_, o_, s_):
            for c, (ia, ni, oa, no, sa, ns) in zip(comms, spans):
                getattr(c, which)(i_[ia:ia + ni], o_[oa:oa + no], s_[sa:sa + ns])
        return run

    return _Comm(ins, outs, sems, each("start"), each("finish"), aliases)


class _ExchangePlan:
    def __init__(self, shards, cidx):
        self.shards, self.cidx = shards, cidx
        self.W, self.G, self.parts, self.pre, self.reduced, self.joined = {}, {}, {}, {}, {}, {}

    def grad(self, key, g):
        self.G[key] = g

    def rider(self, name):
        comms = []
        for kind, names in RIDES.get(name, ()):
            if kind == "gather":
                comms.append(_gather_comm([self.shards[n] for n in names]))
            elif kind == "swap":
                for n in names:
                    self.parts[n] = self._stacked(n)
                comms.append(_swap_comm([self.parts[n] for n in names]))
            elif kind == "exchange":
                comms.append(_exchange_comm([self.pre[n] for n in names]))
            else:
                comms.append(_join_comm([self.reduced[n] for n in names]))
        return _both(comms) if comms else None

    def landed(self, name, outs):
        at = 0
        for kind, names in RIDES[name]:
            for n, o in zip(names, outs[at:at + len(names)]):
                if kind == "gather":
                    self.W[KERNEL_NAME[n]] = self._to_kernel(n, o)
                elif kind == "swap":
                    self.pre[n] = _add_half(self.parts[n], o, self.cidx, name="rs_add_" + _kname(n))
                elif kind == "exchange":
                    self.reduced[n] = _sum_chips(o, self.cidx, name="rs_sum_" + _kname(n))
                else:
                    self.joined[n] = o
            at += len(names)

    @staticmethod
    def _to_kernel(n, stacked):
        if n in GU:
            return stacked
        if n in TRANSPOSED:
            return _w_in_stacked_to_kernel(stacked, W_IN_SHARD)
        full = _assemble(n, stacked)
        return {"w_q_up": _q_up_to_kernel, "w_kv_up": _kv_up_to_kernel}.get(n, lambda w: w)(full)

    def _stacked(self, n):
        g = self.G[_kname(n)]
        if n.partition("@")[0] in GU:
            return g
        if n in TRANSPOSED:
            return _w_in_kernel_to_stacked(g, W_IN_SHARD)
        full = {"w_q_up": _q_up_from_kernel, "w_kv_up": _kv_up_from_kernel}.get(n, lambda w: w)(g)
        return _split(n, full)


def kernel(x, c, positions, w_ada, b_ada, norm_ffn1, w_ffn1_in, w_ffn1_out, norm_mix, w_in, pool_grp, pool_scale, w_pool_proj, q_a_norm, w_q_up, kv_a_norm, w_kv_up, q_norm_nope, q_norm_rope, k_norm_nope, k_norm_rope, w_mla_proj, w_out, norm_ffn2, w_ffn2_in, w_ffn2_out, loss_target, m_w_ada, m_b_ada, m_norm_ffn1, m_w_ffn1_in, m_w_ffn1_out, m_norm_mix, m_w_in, m_pool_grp, m_pool_scale, m_w_pool_proj, m_q_a_norm, m_w_q_up, m_kv_a_norm, m_w_kv_up, m_q_norm_nope, m_q_norm_rope, m_k_norm_nope, m_k_norm_rope, m_w_mla_proj, m_w_out, m_norm_ffn2, m_w_ffn2_in, m_w_ffn2_out, v_w_ada, v_b_ada, v_norm_ffn1, v_w_ffn1_in, v_w_ffn1_out, v_norm_mix, v_w_in, v_pool_grp, v_pool_scale, v_w_pool_proj, v_q_a_norm, v_w_q_up, v_kv_a_norm, v_w_kv_up, v_q_norm_nope, v_q_norm_rope, v_k_norm_nope, v_k_norm_rope, v_w_mla_proj, v_w_out, v_norm_ffn2, v_w_ffn2_in, v_w_ffn2_out):
    args = dict(locals())
    wts = {n: args[n][0] for n in WEIGHTS}
    mom = {n: args["m_" + n][0] for n in WEIGHTS}
    var = {n: args["v_" + n][0] for n in WEIGHTS}
    nb, seq, dm = x.shape
    tokens = nb * seq
    xi, yi, ci = _coords()
    chip = 2 * xi + yi

    cidx = ci.astype(jnp.int32).reshape(1)
    plan = _ExchangePlan({n: _pad_rows(wts[n].T.astype(MX)) if n in TRANSPOSED else wts[n].astype(MX) for n in BIG}, cidx)
    plan.W["pool_grp"] = wts["pool_grp"].astype(MX)

    ncol = w_ada.shape[2]
    b_cols = lax.dynamic_slice_in_dim(wts["b_ada"].reshape(1, -1), chip * ncol, ncol, axis=1)
    c_slots, mod_slots = _run(plan, _ada_prologue, c, wts["w_ada"], b_cols, name="ada_prologue")
    c_all = c_slots[:, :nb].reshape(N_DEV * nb, dm)
    mod3 = jnp.transpose(mod_slots[:, :nb], (1, 0, 2)).reshape(nb, 9, dm)
    P = {"norm_ffn1": wts["norm_ffn1"].reshape(1, dm), "norm_mix": wts["norm_mix"].reshape(1, dm),
         "norm_ffn2": wts["norm_ffn2"].reshape(1, dm), "pool_scale": wts["pool_scale"].reshape(1, POOL_W),
         "q_a_norm": wts["q_a_norm"].reshape(1, QL), "kv_a_norm": wts["kv_a_norm"].reshape(1, KVL),
         "q_gain": _gain_slab(wts["q_norm_nope"].reshape(1, NOPE), wts["q_norm_rope"].reshape(1, ROPE)),
         "k_gain": _gain_slab(wts["k_norm_nope"].reshape(1, NOPE), wts["k_norm_rope"].reshape(1, ROPE))}
    cos, sin = _rope_tables(positions.reshape(tokens))

    loss8, grad_x, stats, d_pool_grp = _layer_fwd_bwd(x.reshape(tokens, dm), loss_target.reshape(tokens, dm), mod3, cos, sin, plan, P)

    slab = _pack_stats(*stats, loss8, name="pack_stats")
    red, dmod_rows, pool_total = _run(plan, _small_allreduce, slab, d_pool_grp.reshape(POOL_G * LANE, LANE), name="small_allreduce")
    grads_t = {n: plan.joined[n][:W_IN_SHARD] for n in TRANSPOSED}
    grads = {n: grads_t[n].T if n in TRANSPOSED else plan.joined[n] for n in BIG if n != "w_ffn1_in"}
    grads["w_ffn1_in"] = jnp.concatenate([plan.joined["w_ffn1_in@0"], plan.joined["w_ffn1_in@1"]], axis=0)
    dm_rows = dmod_rows.shape[0] // N_DEV
    dmod_all = dmod_rows.reshape(N_DEV, dm_rows, SLAB_W)[:, :9 * nb].reshape(N_DEV * nb, 9 * dm)
    dmod_cols = lax.dynamic_slice_in_dim(dmod_all, chip * ncol, ncol, axis=1)
    grads["w_ada"] = _ada_bwd(c_all, dmod_cols, name="ada_bwd")

    delta, new_m, new_v = {}, {}, {}
    as2d = lambda a: a.reshape(POOL_G * LANE, LANE) if a.ndim == 4 else a.reshape(1, -1)
    upd, loss_row = _small_update(red, dmod_rows, pool_total, nb,
                                  {n: tuple(as2d(args[p + n]) for p in ("", "m_", "v_")) for n in SMALL}, name="small_update")
    loss = loss_row[0, 0]
    for n in SMALL:
        grads[n], delta[n], new_m[n], new_v[n] = upd[n]
    for n in ("w_ada",) + BIG:
        if n in NARROW:
            res = _adamw(wts[n].T, grads_t[n] if n in TRANSPOSED else grads[n].T, mom[n].T, var[n].T, name="adamw_" + n)
            delta[n], new_m[n], new_v[n] = (r.T for r in res)
        else:
            delta[n], new_m[n], new_v[n] = _adamw(wts[n], grads[n], mom[n], var[n], name="adamw_" + n)

    def lead(a, n):
        return a.reshape((1,) + wts[n].shape)

    return (loss, grad_x.reshape(nb, seq, dm), *[lead(grads[n], n) for n in WEIGHTS], *[lead(delta[n], n) for n in WEIGHTS],
            *[lead(new_m[n], n) for n in WEIGHTS], *[lead(new_v[n], n) for n in WEIGHTS])
```

```python
import functools
import math

import jax
import jax.numpy as jnp
from jax import lax
from jax.experimental import pallas as pl
from jax.experimental.pallas import tpu as pltpu

F32 = jnp.float32
MX = jnp.bfloat16

D = 1024
DFF = 2816
NH = 8
POOL_W = 512
POOL_G = 4
QL = 384
KVL = 256
ROPE = 32
NOPE = 64
LANE = 128
SUBLANE = 8
EPS = 1e-6
ATTN_SCALE = 1.0 / math.sqrt(96.0)
NEG = -1e30

Z_UP, Z_QL, Z_KV, Z_KR, Z_GP, Z_GM, Z_W = 0, 512, 896, 1152, 1280, 2304, 3328
KR_LANE = 64
LAT_W = 1280

ADAM_LR, ADAM_B1, ADAM_B2, ADAM_EPS, ADAM_WD, ADAM_STEP = 0.001, 0.9, 0.999, 1e-08, 0.01, 10

VMEM_LIMIT = 48 * 1024 * 1024
MESH = pl.DeviceIdType.MESH
N_DEV = 8
N_CHIP = 4


class _Comm:
    def __init__(self, ins, out_shapes, sems, start, finish, aliases=None):
        self.ins, self.out_shapes, self.sems = list(ins), list(out_shapes), list(sems)
        self.start, self.finish = start, finish
        self.aliases = aliases or {}


def _pcall(body, *, name, out_shape, grid=(), in_specs=None, out_specs=None, scratch=(), grid_spec=None, aliases=None,
           comm=None):
    params = pltpu.CompilerParams(vmem_limit_bytes=VMEM_LIMIT)
    kw = dict(name=name, compiler_params=params)
    if comm is None:
        if aliases:
            kw["input_output_aliases"] = aliases
        if grid_spec is not None:
            return pl.pallas_call(body, grid_spec=grid_spec, out_shape=out_shape, **kw)
        return pl.pallas_call(body, grid=grid, in_specs=in_specs, out_specs=out_specs, scratch_shapes=scratch,
                              out_shape=out_shape, **kw)
    single = not isinstance(out_shape, (list, tuple))
    outs = [out_shape] if single else list(out_shape)
    ospecs = [out_specs] if single else list(out_specs)
    n_in, n_out, n_ci, n_co, n_scr = len(in_specs), len(outs), len(comm.ins), len(comm.out_shapes), len(scratch)
    io = dict(aliases or {})
    io.update({n_in + i: n_out + o for i, o in comm.aliases.items()})

    def riding(*refs):
        ins, cins = refs[:n_in], refs[n_in:n_in + n_ci]
        at = n_in + n_ci
        os_, couts = refs[at:at + n_out], refs[at + n_out:at + n_out + n_co]
        at += n_out + n_co
        scr, csems = refs[at:at + n_scr], refs[at + n_scr:]
        if grid:
            first = functools.reduce(jnp.logical_and, [pl.program_id(d) == 0 for d in range(len(grid))])
            last = functools.reduce(jnp.logical_and, [pl.program_id(d) == grid[d] - 1 for d in range(len(grid))])
            pl.when(first)(lambda: comm.start(cins, couts, csems))
            body(*ins, *os_, *scr)
            pl.when(last)(lambda: comm.finish(cins, couts, csems))
        else:
            comm.start(cins, couts, csems)
            body(*ins, *os_, *scr)
            comm.finish(cins, couts, csems)

    call = pl.pallas_call(riding, grid=grid, in_specs=list(in_specs) + [HBM_SPEC] * n_ci, out_specs=ospecs + [HBM_SPEC] * n_co,
                          out_shape=outs + comm.out_shapes, scratch_shapes=list(scratch) + comm.sems,
                          input_output_aliases=io, **kw)

    def run(*args):
        res = call(*args, *comm.ins)
        main = list(res[:n_out])
        return (main[0] if single else main), list(res[n_out:])

    return run


def _pick(dim, target):
    best = None
    for t in range(LANE, min(dim, target) + 1, LANE):
        if dim % t == 0:
            best = t
    return dim if best is None else best


def _sds(shape, dtype):
    return jax.ShapeDtypeStruct(shape, dtype)


def _dw(a, g, *, name, tm=512, tn=1024, out_t=False, comm=None):
    return _mm(a, g, mode="tn", out_dtype=F32, name=name, tm=tm, tn=tn, tk=a.shape[0], n_outer=True, out_t=out_t, comm=comm)


def _mm(a, b, *, mode, out_dtype, name, tm=1024, tn=1024, tk=4096, n_outer=False, out_t=False, comm=None):
    if mode == "nn":
        (M, K), (K2, N) = a.shape, b.shape
    elif mode == "nt":
        (M, K), (N, K2) = a.shape, b.shape
    else:
        (K, M), (K2, N) = a.shape, b.shape
    assert K == K2, (name, a.shape, b.shape)
    tm, tn, tk = _pick(M, tm), _pick(N, tn), _pick(K, tk)
    nk = K // tk
    if n_outer:
        ij = lambda g0, g1: (g1, g0)
        grid = (N // tn, M // tm, nk)
    else:
        ij = lambda g0, g1: (g0, g1)
        grid = (M // tm, N // tn, nk)
    if mode == "tn":
        a_spec = pl.BlockSpec((tk, tm), lambda g0, g1, k: (k, ij(g0, g1)[0]))
    else:
        a_spec = pl.BlockSpec((tm, tk), lambda g0, g1, k: (ij(g0, g1)[0], k))
    if mode == "nt":
        b_spec = pl.BlockSpec((tn, tk), lambda g0, g1, k: (ij(g0, g1)[1], k))
    else:
        b_spec = pl.BlockSpec((tk, tn), lambda g0, g1, k: (k, ij(g0, g1)[1]))
    if out_t:
        assert nk == 1, name
        o_spec = pl.BlockSpec((tn, tm), lambda g0, g1, k: ij(g0, g1)[::-1])
    else:
        o_spec = pl.BlockSpec((tm, tn), lambda g0, g1, k: ij(g0, g1))
    dn = {"nn": (((1,), (0,)), ((), ())), "nt": (((1,), (1,)), ((), ())), "tn": (((0,), (0,)), ((), ()))}[mode]

    def dot(a_ref, b_ref):
        return lax.dot_general(a_ref[...].astype(MX), b_ref[...].astype(MX), dn, preferred_element_type=F32)

    def body_one(a_ref, b_ref, o_ref):
        prod = dot(a_ref, b_ref)
        o_ref[...] = (prod.T if out_t else prod).astype(o_ref.dtype)

    def body_acc(a_ref, b_ref, o_ref, acc_ref):
        k = pl.program_id(2)
        part = dot(a_ref, b_ref)

        @pl.when(k == 0)
        def _():
            acc_ref[...] = part

        @pl.when(k > 0)
        def _():
            acc_ref[...] += part

        @pl.when(k == nk - 1)
        def _():
            o_ref[...] = acc_ref[...].astype(o_ref.dtype)

    return _pcall(body_one if nk == 1 else body_acc, name=name, out_shape=_sds((N, M) if out_t else (M, N), out_dtype), grid=grid,
                  in_specs=[a_spec, b_spec], out_specs=o_spec, scratch=[] if nk == 1 else [pltpu.VMEM((tm, tn), F32)],
                  comm=comm)(a, b)


def _gu_shard(q):
    return (q % 2) * 2 + q // 2


def _ffn_up(h, w_st, *, name, tm=512, comm=None):
    T, dm = h.shape
    hw = w_st.shape[2]
    tm = _pick(T, tm)

    def body(h_ref, wg_ref, wu_ref, gu_ref, a_ref):
        hv = h_ref[...]
        g = jnp.dot(hv, wg_ref[0], preferred_element_type=F32)
        u = jnp.dot(hv, wu_ref[0], preferred_element_type=F32)
        gu_ref[:, :hw] = g.astype(gu_ref.dtype)
        gu_ref[:, hw:] = u.astype(gu_ref.dtype)
        a_ref[...] = (g * _sigmoid(g) * u).astype(a_ref.dtype)

    return _pcall(body, name=name, grid=(T // tm, 2),
                  in_specs=[pl.BlockSpec((tm, dm), lambda i, j: (i, 0)), pl.BlockSpec((1, dm, hw), lambda i, j: (j, 0, 0)),
                            pl.BlockSpec((1, dm, hw), lambda i, j: (2 + j, 0, 0))],
                  out_specs=[pl.BlockSpec((tm, 2 * hw), lambda i, j: (i, j)), pl.BlockSpec((tm, hw), lambda i, j: (i, j))],
                  out_shape=[_sds((T, 4 * hw), MX), _sds((T, 2 * hw), MX)], comm=comm)(h, w_st, w_st)


def _ffn_up_first(x, mod3, gain, w_st, *, sub, name, tm=512, comm=None):
    T, dm = x.shape
    hw = w_st.shape[2]
    tm = _pick(T, tm)
    tps = (T // mod3.shape[0]) // tm

    def body(x_ref, mod_ref, n_ref, wg_ref, wu_ref, h_ref, gu_ref, a_ref):
        xv = x_ref[...]
        xn = xv * _rsq(xv) * n_ref[...]
        hv = (xn * (1.0 + mod_ref[0, 3 * sub + 1:3 * sub + 2, :]) + mod_ref[0, 3 * sub:3 * sub + 1, :]).astype(h_ref.dtype)
        h_ref[...] = hv
        g = jnp.dot(hv, wg_ref[0], preferred_element_type=F32)
        u = jnp.dot(hv, wu_ref[0], preferred_element_type=F32)
        gu_ref[:, :hw] = g.astype(gu_ref.dtype)
        gu_ref[:, hw:] = u.astype(gu_ref.dtype)
        a_ref[...] = (g * _sigmoid(g) * u).astype(a_ref.dtype)

    return _pcall(body, name=name, grid=(T // tm,),
                  in_specs=[_row_spec(tm, dm), pl.BlockSpec((1, 9, dm), lambda i: (i // tps, 0, 0)),
                            pl.BlockSpec((1, dm), lambda i: (0, 0)), pl.BlockSpec((1, dm, hw), lambda i: (0, 0, 0)),
                            pl.BlockSpec((1, dm, hw), lambda i: (2, 0, 0))],
                  out_specs=[_row_spec(tm, dm), pl.BlockSpec((tm, 2 * hw), lambda i: (i, 0)), pl.BlockSpec((tm, hw), lambda i: (i, 0))],
                  out_shape=[_sds((T, dm), MX), _sds((T, 4 * hw), MX), _sds((T, 2 * hw), MX)],
                  comm=comm)(x, mod3, gain, w_st, w_st)


def _ffn_up_second(h, w_st, gu, a, *, name, tm=512, comm=None):
    T, dm = h.shape
    hw = w_st.shape[2]
    tm = _pick(T, tm)

    def body(h_ref, wg_ref, wu_ref, gu_in, a_in, gu_ref, a_ref):
        hv = h_ref[...]
        g = jnp.dot(hv, wg_ref[0], preferred_element_type=F32)
        u = jnp.dot(hv, wu_ref[0], preferred_element_type=F32)
        gu_ref[:, :hw] = g.astype(gu_ref.dtype)
        gu_ref[:, hw:] = u.astype(gu_ref.dtype)
        a_ref[...] = (g * _sigmoid(g) * u).astype(a_ref.dtype)

    return _pcall(body, name=name, grid=(T // tm,),
                  in_specs=[_row_spec(tm, dm), pl.BlockSpec((1, dm, hw), lambda i: (1, 0, 0)),
                            pl.BlockSpec((1, dm, hw), lambda i: (3, 0, 0)), HBM_SPEC, HBM_SPEC],
                  out_specs=[pl.BlockSpec((tm, 2 * hw), lambda i: (i, 1)), pl.BlockSpec((tm, hw), lambda i: (i, 1))],
                  out_shape=[_sds(gu.shape, gu.dtype), _sds(a.shape, a.dtype)], aliases={3: 0, 4: 1},
                  comm=comm)(h, w_st, w_st, gu, a)


def _ffn_dact(df, gu, w_out, *, name, tm=512, comm=None):
    T, dm = df.shape
    hw = gu.shape[1] // 4
    tm = _pick(T, tm)

    def body(df_ref, gu_ref, wo_ref, dgu_ref):
        da = lax.dot_general(df_ref[...], wo_ref[...], (((1,), (1,)), ((), ())), preferred_element_type=F32)
        g = gu_ref[:, :hw].astype(F32)
        u = gu_ref[:, hw:].astype(F32)
        s = _sigmoid(g)
        dgu_ref[:, :hw] = (da * u * (s * (1.0 + g * (1.0 - s)))).astype(dgu_ref.dtype)
        dgu_ref[:, hw:] = (da * (g * s)).astype(dgu_ref.dtype)

    return _pcall(body, name=name, grid=(T // tm, 2),
                  in_specs=[pl.BlockSpec((tm, dm), lambda i, j: (i, 0)), pl.BlockSpec((tm, 2 * hw), lambda i, j: (i, j)),
                            pl.BlockSpec((hw, dm), lambda i, j: (j, 0))],
                  out_specs=pl.BlockSpec((tm, 2 * hw), lambda i, j: (i, j)), out_shape=_sds(gu.shape, MX),
                  comm=comm)(df, gu, w_out)


def _ffn_dh(dgu, w_st, *, name, tm=1024, comm=None):
    T = dgu.shape[0]
    _, dm, hw = w_st.shape
    tm = _pick(T, tm)

    def body(d_ref, w_ref, o_ref, acc_ref):
        q = pl.program_id(1)
        part = lax.dot_general(d_ref[...], w_ref[0], (((1,), (1,)), ((), ())), preferred_element_type=F32)

        @pl.when(q == 0)
        def _():
            acc_ref[...] = part

        @pl.when(jnp.logical_and(q > 0, q < 3))
        def _():
            acc_ref[...] += part

        @pl.when(q == 3)
        def _():
            o_ref[...] = acc_ref[...] + part

    return _pcall(body, name=name, grid=(T // tm, 4),
                  in_specs=[pl.BlockSpec((tm, hw), lambda i, q: (i, q)), pl.BlockSpec((1, dm, hw), lambda i, q: (_gu_shard(q), 0, 0))],
                  out_specs=pl.BlockSpec((tm, dm), lambda i, q: (i, 0)), out_shape=_sds((T, dm), F32),
                  scratch=[pltpu.VMEM((tm, dm), F32)], comm=comm)(dgu, w_st)


def _ffn_dw_in(h, dgu, *, name, rows=None, tm=512, comm=None):
    T, dm = h.shape
    hw = dgu.shape[1] // 4
    first, count = rows if rows is not None else (0, dm)
    tm = _pick(count, tm)
    skip = first // tm

    def body(h_ref, d_ref, o_ref):
        o_ref[0] = lax.dot_general(h_ref[...], d_ref[...], (((0,), (0,)), ((), ())), preferred_element_type=F32)

    return _pcall(body, name=name, grid=(4, count // tm),
                  in_specs=[pl.BlockSpec((T, tm), lambda q, i: (0, skip + i)), pl.BlockSpec((T, hw), lambda q, i: (0, q))],
                  out_specs=pl.BlockSpec((1, tm, hw), lambda q, i: (_gu_shard(q), i, 0)),
                  out_shape=_sds((4, count, hw), F32), comm=comm)(h, dgu)


def _rsq(x):
    return lax.rsqrt(jnp.mean(x * x, axis=-1, keepdims=True) + EPS)


def _sigmoid(x):
    return 1.0 / (1.0 + jnp.exp(-x))


def _row_spec(tm, w):
    return pl.BlockSpec((tm, w), lambda i: (i, 0))


def _mm_resid_norm(a, w, x_prev, mod3, gain, *, sub, coef, name, tm=512, comm=None):
    T, k = a.shape
    dm = w.shape[1]
    tm = _pick(T, tm)
    tps = (T // mod3.shape[0]) // tm

    def body(a_ref, w_ref, x_ref, mod_ref, n_ref, f_ref, xo_ref, h_ref):
        f = jnp.dot(a_ref[...], w_ref[...], preferred_element_type=F32)
        f_ref[...] = f
        x = x_ref[...] + coef * mod_ref[0, 3 * sub - 1:3 * sub, :] * f
        xo_ref[...] = x
        xn = x * _rsq(x) * n_ref[...]
        h_ref[...] = (xn * (1.0 + mod_ref[0, 3 * sub + 1:3 * sub + 2, :]) + mod_ref[0, 3 * sub:3 * sub + 1, :]).astype(h_ref.dtype)

    row = _row_spec(tm, dm)
    return _pcall(body, name=name, grid=(T // tm,),
                  in_specs=[_row_spec(tm, k), pl.BlockSpec((k, dm), lambda i: (0, 0)), row,
                            pl.BlockSpec((1, 9, dm), lambda i: (i // tps, 0, 0)), pl.BlockSpec((1, dm), lambda i: (0, 0))],
                  out_specs=[row, row, row], out_shape=[_sds((T, dm), F32), _sds((T, dm), F32), _sds((T, dm), MX)],
                  comm=comm)(a, w, x_prev, mod3, gain)


def _mm_loss(a, w, x2, tgt, mod3, *, name, tm=512):
    T, k = a.shape
    dm = w.shape[1]
    tm = _pick(T, tm)
    tps = (T // mod3.shape[0]) // tm
    mod_spec = pl.BlockSpec((1, 9, dm), lambda i: (i // tps, 0, 0))
    row = _row_spec(tm, dm)
    stat_spec = pl.BlockSpec((1, SUBLANE, dm), lambda i: (i // tps, 0, 0))
    loss_spec = pl.BlockSpec((SUBLANE, LANE), lambda i: (0, 0))

    def body(a_ref, w_ref, x_ref, t_ref, mod_ref, dy_ref, df_ref, st_ref, loss_ref):
        i = pl.program_id(0)
        g = mod_ref[0, 8:9, :]
        f = jnp.dot(a_ref[...], w_ref[...], preferred_element_type=F32)
        err = x_ref[...] + 0.5 * g * f - t_ref[...]
        dy = err * (1.0 / dm)
        dy_ref[...] = dy
        df_ref[...] = (0.5 * g * dy).astype(df_ref.dtype)
        dgate = jnp.sum(0.5 * dy * f, axis=0, keepdims=True)
        part = 0.5 * jnp.sum(jnp.sum(err * err, axis=0, keepdims=True), axis=1, keepdims=True) * (1.0 / dm)

        @pl.when(i % tps == 0)
        def _():
            st_ref[...] = jnp.zeros_like(st_ref)

        @pl.when(i == 0)
        def _():
            loss_ref[...] = jnp.zeros_like(loss_ref)

        st_ref[0, 0:1, :] += dgate
        loss_ref[...] += jnp.broadcast_to(part, loss_ref.shape)

    return _pcall(body, name=name, grid=(T // tm,),
                  in_specs=[_row_spec(tm, k), pl.BlockSpec((k, dm), lambda i: (0, 0)), row, row, mod_spec],
                  out_specs=[row, row, stat_spec, loss_spec],
                  out_shape=[_sds((T, dm), F32), _sds((T, dm), MX), _sds((mod3.shape[0], SUBLANE, dm), F32),
                             _sds((SUBLANE, LANE), F32)])(a, w, x2, tgt, mod3)


def _norm_bwd_tail(dhv, x_ref, dxi_ref, f_ref, mod_ref, n_ref, dx_ref, df_ref, st_ref, *, first, sub, coef):
    x = x_ref[...]
    r = _rsq(x)
    xhat = x * r
    n = n_ref[...]
    d_shift = jnp.sum(dhv, axis=0, keepdims=True)
    d_scale = jnp.sum(dhv * (xhat * n), axis=0, keepdims=True)
    dxn = dhv * (1.0 + mod_ref[0, 3 * sub + 1:3 * sub + 2, :])
    d_gain = jnp.sum(dxn * xhat, axis=0, keepdims=True)
    dxhat = dxn * n
    dx = dxi_ref[...] + r * (dxhat - xhat * jnp.mean(dxhat * xhat, axis=-1, keepdims=True))
    dx_ref[...] = dx

    @pl.when(first)
    def _():
        st_ref[...] = jnp.zeros_like(st_ref)

    st_ref[0, 0:1, :] += d_shift
    st_ref[0, 1:2, :] += d_scale
    st_ref[0, 2:3, :] += d_gain
    if f_ref is not None:
        st_ref[0, 3:4, :] += jnp.sum(coef * dx * f_ref[...], axis=0, keepdims=True)
        df_ref[...] = (coef * mod_ref[0, 3 * sub - 1:3 * sub, :] * dx).astype(df_ref.dtype)


def _ffn_dh_norm(dgu, w_st, x_cur, dx_in, f_prev, mod3, gain, *, sub, coef, name, tm=512, comm=None):
    T = dgu.shape[0]
    _, dm, hw = w_st.shape
    tm = _pick(T, tm)
    nb = mod3.shape[0]
    tps = (T // nb) // tm

    def body(d_ref, w_ref, x_ref, dxi_ref, f_ref, mod_ref, n_ref, dx_ref, df_ref, st_ref, acc_ref):
        i = pl.program_id(0)
        q = pl.program_id(1)
        part = lax.dot_general(d_ref[...], w_ref[0], (((1,), (1,)), ((), ())), preferred_element_type=F32)

        @pl.when(q == 0)
        def _():
            acc_ref[...] = part

        @pl.when(q > 0)
        def _():
            acc_ref[...] += part

        @pl.when(q == 3)
        def _():
            _norm_bwd_tail(acc_ref[...], x_ref, dxi_ref, f_ref, mod_ref, n_ref, dx_ref, df_ref, st_ref,
                           first=i % tps == 0, sub=sub, coef=coef)

    row = pl.BlockSpec((tm, dm), lambda i, q: (i, 0))
    return _pcall(body, name=name, grid=(T // tm, 4),
                  in_specs=[pl.BlockSpec((tm, hw), lambda i, q: (i, q)), pl.BlockSpec((1, dm, hw), lambda i, q: (_gu_shard(q), 0, 0)),
                            row, row, row, pl.BlockSpec((1, 9, dm), lambda i, q: (i // tps, 0, 0)),
                            pl.BlockSpec((1, dm), lambda i, q: (0, 0))],
                  out_specs=[row, row, pl.BlockSpec((1, SUBLANE, dm), lambda i, q: (i // tps, 0, 0))],
                  out_shape=[_sds((T, dm), F32), _sds((T, dm), MX), _sds((nb, SUBLANE, dm), F32)],
                  scratch=[pltpu.VMEM((tm, dm), F32)], comm=comm)(dgu, w_st, x_cur, dx_in, f_prev, mod3, gain)


def _mm_norm_bwd(a, b, x_cur, dx_in, f_prev, mod3, gain, *, sub, coef, name, tm=512, comm=None):
    T, k = a.shape
    dm = b.shape[1]
    tm = _pick(T, tm)
    nb = mod3.shape[0]
    tps = (T // nb) // tm

    def body(a_ref, b_ref, x_ref, dxi_ref, f_ref, mod_ref, n_ref, dx_ref, df_ref, st_ref):
        dh = jnp.dot(a_ref[...], b_ref[...], preferred_element_type=F32)
        _norm_bwd_tail(dh, x_ref, dxi_ref, f_ref, mod_ref, n_ref, dx_ref, df_ref, st_ref,
                       first=pl.program_id(0) % tps == 0, sub=sub, coef=coef)

    row = _row_spec(tm, dm)
    return _pcall(body, name=name, grid=(T // tm,),
                  in_specs=[_row_spec(tm, k), pl.BlockSpec((k, dm), lambda i: (0, 0)), row, row, row,
                            pl.BlockSpec((1, 9, dm), lambda i: (i // tps, 0, 0)), pl.BlockSpec((1, dm), lambda i: (0, 0))],
                  out_specs=[row, row, pl.BlockSpec((1, SUBLANE, dm), lambda i: (i // tps, 0, 0))],
                  out_shape=[_sds((T, dm), F32), _sds((T, dm), MX), _sds((nb, SUBLANE, dm), F32)],
                  comm=comm)(a, b, x_cur, dx_in, f_prev, mod3, gain)


def _bwd_block(x_cur, dh, dx_in, f_prev, mod3, gain, *, sub, coef, name, tm=256, comm=None):
    T, dm = x_cur.shape
    nb = mod3.shape[0]
    tps = (T // nb) // tm
    has_f = f_prev is not None
    mod_spec = pl.BlockSpec((1, 9, dm), lambda i: (i // tps, 0, 0))
    vec_spec = pl.BlockSpec((1, dm), lambda i: (0, 0))
    stat_spec = pl.BlockSpec((1, SUBLANE, dm), lambda i: (i // tps, 0, 0))
    row = _row_spec(tm, dm)

    def body(*refs):
        if has_f:
            x_ref, dh_ref, dxi_ref, f_ref, mod_ref, n_ref, dx_ref, df_ref, st_ref = refs
        else:
            x_ref, dh_ref, dxi_ref, mod_ref, n_ref, dx_ref, st_ref = refs
            f_ref = df_ref = None
        _norm_bwd_tail(dh_ref[...], x_ref, dxi_ref, f_ref, mod_ref, n_ref, dx_ref, df_ref, st_ref,
                       first=pl.program_id(0) % tps == 0, sub=sub, coef=coef)

    st_shape = _sds((nb, SUBLANE, dm), F32)
    if has_f:
        return _pcall(body, name=name, grid=(T // tm,), in_specs=[row, row, row, row, mod_spec, vec_spec],
                      out_specs=[row, row, stat_spec],
                      out_shape=[_sds((T, dm), F32), _sds((T, dm), MX), st_shape], comm=comm)(x_cur, dh, dx_in, f_prev, mod3, gain)
    return _pcall(body, name=name, grid=(T // tm,), in_specs=[row, row, row, mod_spec, vec_spec],
                  out_specs=[row, stat_spec], out_shape=[_sds((T, dm), F32), st_shape], comm=comm)(x_cur, dh, dx_in, mod3, gain)


def _mix_out_fwd(z, br_pool, attn, w_mla, w_out, x_prev, mod3, gain, *, sub, name, tm=512):
    T = z.shape[0]
    dm = w_out.shape[1]
    tm = _pick(T, tm)
    tps = (T // mod3.shape[0]) // tm
    whole = lambda a: pl.BlockSpec(a.shape, lambda i: (0, 0))

    def body(z_ref, bp_ref, at_ref, wm_ref, wo_ref, x_ref, mod_ref, n_ref, bm_ref, mg_ref, mo_ref, xo_ref, h_ref):
        bm = jnp.dot(at_ref[...], wm_ref[...], preferred_element_type=F32).astype(MX)
        bm_ref[...] = bm
        gp = z_ref[:, Z_GP:Z_GP + D].astype(F32)
        gm = z_ref[:, Z_GM:Z_GM + D].astype(F32)
        merged = (_sigmoid(gp) * bp_ref[...] + _sigmoid(gm) * bm).astype(MX)
        mg_ref[...] = merged
        mo = jnp.dot(merged, wo_ref[...], preferred_element_type=F32)
        mo_ref[...] = mo
        x = x_ref[...] + mod_ref[0, 3 * sub - 1:3 * sub, :] * mo
        xo_ref[...] = x
        xn = x * _rsq(x) * n_ref[...]
        h_ref[...] = (xn * (1.0 + mod_ref[0, 3 * sub + 1:3 * sub + 2, :]) + mod_ref[0, 3 * sub:3 * sub + 1, :]).astype(h_ref.dtype)

    row = _row_spec(tm, dm)
    return _pcall(body, name=name, grid=(T // tm,),
                  in_specs=[_row_spec(tm, Z_W), row, _row_spec(tm, attn.shape[1]), whole(w_mla), whole(w_out), row,
                            pl.BlockSpec((1, 9, dm), lambda i: (i // tps, 0, 0)), pl.BlockSpec((1, dm), lambda i: (0, 0))],
                  out_specs=[row, row, row, row, row],
                  out_shape=[_sds((T, dm), MX), _sds((T, dm), MX), _sds((T, dm), F32), _sds((T, dm), F32), _sds((T, dm), MX)],
                  )(z, br_pool, attn, w_mla, w_out, x_prev, mod3, gain)


def _mix_out_bwd(dmo, w_out, z, br_pool, br_mla, w_mla, *, name, tm=512):
    T = z.shape[0]
    tm = _pick(T, tm)
    whole = lambda a: pl.BlockSpec(a.shape, lambda i: (0, 0))
    nt = (((1,), (1,)), ((), ()))

    def body(dmo_ref, wo_ref, z_ref, bp_ref, bm_ref, wm_ref, dbp_ref, dbm_ref, dg_ref, dat_ref):
        dm = lax.dot_general(dmo_ref[...], wo_ref[...], nt, preferred_element_type=F32)
        sp = _sigmoid(z_ref[:, Z_GP:Z_GP + D].astype(F32))
        sm = _sigmoid(z_ref[:, Z_GM:Z_GM + D].astype(F32))
        dbp_ref[...] = (dm * sp).astype(dbp_ref.dtype)
        dbm = (dm * sm).astype(MX)
        dbm_ref[...] = dbm
        dg_ref[:, :D] = (dm * bp_ref[...].astype(F32) * sp * (1.0 - sp)).astype(dg_ref.dtype)
        dg_ref[:, D:] = (dm * bm_ref[...].astype(F32) * sm * (1.0 - sm)).astype(dg_ref.dtype)
        dat_ref[...] = lax.dot_general(dbm, wm_ref[...], nt, preferred_element_type=F32)

    row = _row_spec(tm, D)
    return _pcall(body, name=name, grid=(T // tm,),
                  in_specs=[row, whole(w_out), _row_spec(tm, Z_W), row, row, whole(w_mla)],
                  out_specs=[row, row, _row_spec(tm, 2 * D), _row_spec(tm, w_mla.shape[0])],
                  out_shape=[_sds((T, D), MX), _sds((T, D), MX), _sds((T, 2 * D), MX), _sds((T, w_mla.shape[0]), F32)],
                  )(dmo, w_out, z, br_pool, br_mla, w_mla)


def _shift_down(x, k, row):
    return jnp.where(row >= k, pltpu.roll(x, k, 0), 0.0)


def _shift_up(x, k, row, n):
    return jnp.where(row < n - k, pltpu.roll(x, n - k, 0), 0.0)


def _pool_fwd(z, pool_grp, pool_scale, *, nb, name):
    T = z.shape[0]
    S = T // nb
    blk = pl.BlockSpec((S, LANE), lambda b, g: (b, g))

    def body(u_ref, w_ref, s_ref, pooled_ref, mixed_ref, scaled_ref):
        g = pl.program_id(1)
        u = u_ref[...].astype(F32)
        row = lax.broadcasted_iota(jnp.int32, u.shape, 0)
        s2 = u + _shift_down(u, 1, row)
        s4 = s2 + _shift_down(s2, 2, row)
        s8 = s4 + _shift_down(s4, 4, row)
        s16 = s8 + _shift_down(s8, 8, row)
        win = jnp.where(g == 0, s2, jnp.where(g == 1, s4, jnp.where(g == 2, s8, s16)))
        width = lax.shift_left(jnp.int32(2), g)
        cnt = jnp.minimum(row + 1, width).astype(F32)
        pooled = (win / cnt - u).astype(MX)
        pooled_ref[...] = pooled
        mixed = jnp.dot(pooled, w_ref[0], preferred_element_type=F32)
        mixed_ref[...] = mixed
        scaled_ref[...] = (mixed * s_ref[...]).astype(scaled_ref.dtype)

    return _pcall(body, name=name, grid=(nb, POOL_G),
                  in_specs=[blk, pl.BlockSpec((1, LANE, LANE), lambda b, g: (g, 0, 0)),
                            pl.BlockSpec((1, LANE), lambda b, g: (0, g))],
                  out_specs=[blk, blk, blk],
                  out_shape=[_sds((T, POOL_W), MX), _sds((T, POOL_W), F32), _sds((T, POOL_W), MX)])(z, pool_grp, pool_scale)


def _pool_bwd(dscaled, mixed, pooled, pool_grp, pool_scale, *, nb, name):
    T = dscaled.shape[0]
    S = T // nb
    blk = pl.BlockSpec((S, LANE), lambda g, b: (b, g))

    def body(ds_ref, mixed_ref, pooled_ref, w_ref, s_ref, du_ref, dw_ref, dsc_ref):
        g = pl.program_id(0)
        b = pl.program_id(1)
        ds = ds_ref[...]
        dsc_ref[0] = jnp.sum(ds * mixed_ref[...], axis=0, keepdims=True)
        dmixed = (ds * s_ref[...]).astype(MX)
        dw = lax.dot_general(pooled_ref[...], dmixed, (((0,), (0,)), ((), ())), preferred_element_type=F32)

        @pl.when(b == 0)
        def _():
            dw_ref[0] = dw

        @pl.when(b > 0)
        def _():
            dw_ref[0] += dw
        dpooled = lax.dot_general(dmixed, w_ref[0], (((1,), (1,)), ((), ())), preferred_element_type=F32)
        row = lax.broadcasted_iota(jnp.int32, dpooled.shape, 0)
        width = lax.shift_left(jnp.int32(2), g)
        q = dpooled / jnp.minimum(row + 1, width).astype(F32)
        r2 = q + _shift_up(q, 1, row, S)
        r4 = r2 + _shift_up(r2, 2, row, S)
        r8 = r4 + _shift_up(r4, 4, row, S)
        r16 = r8 + _shift_up(r8, 8, row, S)
        win = jnp.where(g == 0, r2, jnp.where(g == 1, r4, jnp.where(g == 2, r8, r16)))
        du_ref[...] = (win - dpooled).astype(du_ref.dtype)

    return _pcall(body, name=name, grid=(POOL_G, nb),
                  in_specs=[blk, blk, blk, pl.BlockSpec((1, LANE, LANE), lambda g, b: (g, 0, 0)),
                            pl.BlockSpec((1, LANE), lambda g, b: (0, g))],
                  out_specs=[blk, pl.BlockSpec((1, LANE, LANE), lambda g, b: (g, 0, 0)),
                             pl.BlockSpec((1, 1, LANE), lambda g, b: (b, 0, g))],
                  out_shape=[_sds((T, POOL_W), MX), _sds((POOL_G, LANE, LANE), F32), _sds((nb, 1, POOL_W), F32)],
                  )(dscaled, mixed, pooled, pool_grp, pool_scale)


def _lane_masks(shape):
    lane = lax.broadcasted_iota(jnp.int32, shape, len(shape) - 1)
    m_n = lane < NOPE
    m_r = jnp.logical_and(lane >= KR_LANE, lane < KR_LANE + ROPE)
    first_half = lane < KR_LANE + ROPE // 2
    return m_n, m_r, first_half


def _rot(y, first_half):
    return jnp.where(first_half, -pltpu.roll(y, LANE - ROPE // 2, 1), pltpu.roll(y, ROPE // 2, 1))


def _rot_t(v, first_half, m_r):
    return jnp.where(m_r, jnp.where(first_half, pltpu.roll(v, LANE - ROPE // 2, 1), -pltpu.roll(v, ROPE // 2, 1)), 0.0)


def _mla_in_fwd(z, w_q, w_kv, qa_gain, kva_gain, cos, sin, q_gain, k_gain, *, name, tm=256):
    T = z.shape[0]
    slab = pl.BlockSpec((tm, LANE), lambda i: (i, 0))
    vec = pl.BlockSpec((1, LANE), lambda i: (0, 0))
    whole = lambda a: pl.BlockSpec(a.shape, lambda i: (0, 0))

    def body(z_ref, wq_ref, wkv_ref, qa_ref, kva_ref, cos_ref, sin_ref, qg_ref, kg_ref,
             qn_ref, kvn_ref, qp_ref, kvp_ref, q_ref, k_ref, v_ref):
        ql = z_ref[:, Z_QL:Z_QL + QL].astype(F32)
        kvl = z_ref[:, Z_KV:Z_KV + KVL].astype(F32)
        qn = (ql * _rsq(ql) * qa_ref[...]).astype(MX)
        kvn = (kvl * _rsq(kvl) * kva_ref[...]).astype(MX)
        qn_ref[...] = qn
        kvn_ref[...] = kvn
        qp = jnp.dot(qn, wq_ref[...], preferred_element_type=F32)
        kvp = jnp.dot(kvn, wkv_ref[...], preferred_element_type=F32)
        qp_ref[...] = qp
        kvp_ref[...] = kvp
        m_n, m_r, first_half = _lane_masks((tm, LANE))
        c = cos_ref[...]
        s = sin_ref[...]
        qg = qg_ref[...]
        kg = kg_ref[...]
        xr = z_ref[:, Z_KR:Z_KR + LANE].astype(F32)
        rr = lax.rsqrt(jnp.sum(xr * xr, axis=-1, keepdims=True) * (1.0 / ROPE) + EPS)
        yr = xr * rr * kg
        kr = jnp.where(m_r, yr * c + _rot(yr, first_half) * s, 0.0)
        for h in range(NH):
            x = qp[:, h * LANE:(h + 1) * LANE]
            x2 = x * x
            rn = lax.rsqrt(jnp.sum(jnp.where(m_n, x2, 0.0), axis=-1, keepdims=True) * (1.0 / NOPE) + EPS)
            rq = lax.rsqrt(jnp.sum(jnp.where(m_r, x2, 0.0), axis=-1, keepdims=True) * (1.0 / ROPE) + EPS)
            y = x * jnp.where(m_n, rn, jnp.where(m_r, rq, 0.0)) * qg
            q_ref[:, h * LANE:(h + 1) * LANE] = ((y * c + _rot(y, first_half) * s) * ATTN_SCALE).astype(q_ref.dtype)
            xk = kvp[:, h * LANE:(h + 1) * LANE]
            rk = lax.rsqrt(jnp.sum(jnp.where(m_n, xk * xk, 0.0), axis=-1, keepdims=True) * (1.0 / NOPE) + EPS)
            k_ref[:, h * LANE:(h + 1) * LANE] = (jnp.where(m_n, xk * rk * kg, 0.0) + kr).astype(k_ref.dtype)
        v_ref[...] = kvp[:, NH * LANE:].astype(v_ref.dtype)

    return _pcall(body, name=name, grid=(T // tm,),
                  in_specs=[_row_spec(tm, LAT_W), whole(w_q), whole(w_kv), whole(qa_gain), whole(kva_gain), slab, slab, vec, vec],
                  out_specs=[_row_spec(tm, QL), _row_spec(tm, KVL), _row_spec(tm, NH * LANE), _row_spec(tm, NH * LANE + NH * NOPE),
                             _row_spec(tm, NH * LANE), _row_spec(tm, NH * LANE), _row_spec(tm, NH * NOPE)],
                  out_shape=[_sds((T, QL), MX), _sds((T, KVL), MX), _sds((T, NH * LANE), F32), _sds((T, NH * LANE + NH * NOPE), F32),
                             _sds((T, NH * LANE), MX), _sds((T, NH * LANE), MX), _sds((T, NH * NOPE), MX)],
                  )(z, w_q, w_kv, qa_gain, kva_gain, cos, sin, q_gain, k_gain)


def _mla_in_bwd(dq, dk, dv, qp, kvp, z, w_q, w_kv, qa_gain, kva_gain, cos, sin, q_gain, k_gain, *, nb, name, tm=256):
    T = dq.shape[0]
    tps = (T // nb) // tm
    slab = pl.BlockSpec((tm, LANE), lambda i: (i, 0))
    vec = pl.BlockSpec((1, LANE), lambda i: (0, 0))
    whole = lambda a: pl.BlockSpec(a.shape, lambda i: (0, 0))

    def latent_bwd(x, dy, gain):
        r = _rsq(x)
        xhat = x * r
        dxhat = dy * gain
        return r * (dxhat - xhat * jnp.mean(dxhat * xhat, axis=-1, keepdims=True)), jnp.sum(dy * xhat, axis=0, keepdims=True)

    def body(dq_ref, dk_ref, dv_ref, qp_ref, kvp_ref, z_ref, wq_ref, wkv_ref, qa_ref, kva_ref, cos_ref, sin_ref, qg_ref, kg_ref,
             dqp_ref, dkvp_ref, dkr_ref, dql_ref, dkvl_ref, st_ref, sq_ref, sk_ref):
        i = pl.program_id(0)
        qp, kvp = qp_ref, kvp_ref
        m_n, m_r, first_half = _lane_masks((tm, LANE))
        c = cos_ref[...]
        s = sin_ref[...]
        qg = qg_ref[...]
        kg = kg_ref[...]
        dqg = jnp.zeros((1, LANE), F32)
        dkg = jnp.zeros((1, LANE), F32)
        dkr_sum = jnp.zeros((tm, LANE), F32)
        for h in range(NH):
            x = qp[:, h * LANE:(h + 1) * LANE]
            x2 = x * x
            rn = lax.rsqrt(jnp.sum(jnp.where(m_n, x2, 0.0), axis=-1, keepdims=True) * (1.0 / NOPE) + EPS)
            rq = lax.rsqrt(jnp.sum(jnp.where(m_r, x2, 0.0), axis=-1, keepdims=True) * (1.0 / ROPE) + EPS)
            rfac = jnp.where(m_n, rn, jnp.where(m_r, rq, 0.0))
            xhat = x * rfac
            do = dq_ref[:, h * LANE:(h + 1) * LANE] * ATTN_SCALE
            dy = do * c + _rot_t(do * s, first_half, m_r)
            dqg = dqg + jnp.sum(dy * xhat, axis=0, keepdims=True)
            dxhat = dy * qg
            t = dxhat * xhat
            mean_n = jnp.sum(jnp.where(m_n, t, 0.0), axis=-1, keepdims=True) * (1.0 / NOPE)
            mean_r = jnp.sum(jnp.where(m_r, t, 0.0), axis=-1, keepdims=True) * (1.0 / ROPE)
            dqp_ref[:, h * LANE:(h + 1) * LANE] = (
                rfac * (dxhat - xhat * jnp.where(m_n, mean_n, jnp.where(m_r, mean_r, 0.0)))).astype(dqp_ref.dtype)

            xk = kvp[:, h * LANE:(h + 1) * LANE]
            rk = lax.rsqrt(jnp.sum(jnp.where(m_n, xk * xk, 0.0), axis=-1, keepdims=True) * (1.0 / NOPE) + EPS)
            khat = jnp.where(m_n, xk * rk, 0.0)
            dko = dk_ref[:, h * LANE:(h + 1) * LANE]
            dkn = jnp.where(m_n, dko, 0.0)
            dkg = dkg + jnp.sum(dkn * khat, axis=0, keepdims=True)
            dkhat = dkn * kg
            mean_k = jnp.sum(dkhat * khat, axis=-1, keepdims=True) * (1.0 / NOPE)
            dkvp_ref[:, h * LANE:(h + 1) * LANE] = jnp.where(m_n, rk * (dkhat - khat * mean_k), 0.0).astype(dkvp_ref.dtype)
            dkr_sum = dkr_sum + jnp.where(m_r, dko, 0.0)
        dkvp_ref[:, NH * LANE:] = dv_ref[...].astype(dkvp_ref.dtype)

        xr = z_ref[:, Z_KR:Z_KR + LANE].astype(F32)
        rr = lax.rsqrt(jnp.sum(xr * xr, axis=-1, keepdims=True) * (1.0 / ROPE) + EPS)
        rhat = xr * rr
        dyr = dkr_sum * c + _rot_t(dkr_sum * s, first_half, m_r)
        dkg = dkg + jnp.sum(dyr * rhat, axis=0, keepdims=True)
        drhat = dyr * kg
        mean_kr = jnp.sum(drhat * rhat, axis=-1, keepdims=True) * (1.0 / ROPE)
        dkr_ref[...] = jnp.where(m_r, rr * (drhat - rhat * mean_kr), 0.0).astype(dkr_ref.dtype)

        nt = (((1,), (1,)), ((), ()))
        dqn = lax.dot_general(dqp_ref[...], wq_ref[...], nt, preferred_element_type=F32)
        dkvn = lax.dot_general(dkvp_ref[...], wkv_ref[...], nt, preferred_element_type=F32)
        dql, dqa = latent_bwd(z_ref[:, Z_QL:Z_QL + QL].astype(F32), dqn, qa_ref[...])
        dkvl, dkva = latent_bwd(z_ref[:, Z_KV:Z_KV + KVL].astype(F32), dkvn, kva_ref[...])
        dql_ref[...] = dql.astype(dql_ref.dtype)
        dkvl_ref[...] = dkvl.astype(dkvl_ref.dtype)

        @pl.when(i % tps == 0)
        def _():
            st_ref[...] = jnp.zeros_like(st_ref)
            sq_ref[...] = jnp.zeros_like(sq_ref)
            sk_ref[...] = jnp.zeros_like(sk_ref)

        st_ref[0, 0:1, :] += dqg
        st_ref[0, 1:2, :] += dkg
        sq_ref[0, 0:1, :] += dqa
        sk_ref[0, 0:1, :] += dkva

    stat = lambda w: pl.BlockSpec((1, SUBLANE, w), lambda i: (i // tps, 0, 0))
    return _pcall(body, name=name, grid=(T // tm,),
                  in_specs=[_row_spec(tm, NH * LANE), _row_spec(tm, NH * LANE), _row_spec(tm, NH * NOPE),
                            _row_spec(tm, NH * LANE), _row_spec(tm, NH * LANE + NH * NOPE), _row_spec(tm, LAT_W),
                            whole(w_q), whole(w_kv), whole(qa_gain), whole(kva_gain), slab, slab, vec, vec],
                  out_specs=[_row_spec(tm, NH * LANE), _row_spec(tm, NH * LANE + NH * NOPE), slab, _row_spec(tm, QL),
                             _row_spec(tm, KVL), stat(LANE), stat(QL), stat(KVL)],
                  out_shape=[_sds((T, NH * LANE), MX), _sds((T, NH * LANE + NH * NOPE), MX), _sds((T, LANE), MX),
                             _sds((T, QL), MX), _sds((T, KVL), MX), _sds((nb, SUBLANE, LANE), F32),
                             _sds((nb, SUBLANE, QL), F32), _sds((nb, SUBLANE, KVL), F32)],
                  )(dq, dk, dv, qp, kvp, z, w_q, w_kv, qa_gain, kva_gain, cos, sin, q_gain, k_gain)


def _lower_triangle(t):
    return lax.broadcasted_iota(jnp.int32, (t, t), 1) <= lax.broadcasted_iota(jnp.int32, (t, t), 0)


def _attn_fwd(q, k, v, *, nb, name, tq=512, comm=None):
    T = q.shape[0]
    S = T // nb
    tq = _pick(S, tq)
    nq = S // tq
    tk = tq
    npair = NH // 2

    def body(q_ref, k_ref, v_ref, o_ref, lse_ref):
        qi = pl.program_id(2)
        lane = lax.broadcasted_iota(jnp.int32, (tq, LANE), 1)
        qs = [q_ref[:, hh * LANE:(hh + 1) * LANE] for hh in range(2)]

        def block(j, carry, diagonal):
            k0 = pl.multiple_of(j * tk, tk)
            vb = v_ref[pl.ds(k0, tk), :]
            new = []
            for hh in range(2):
                m, l, acc = carry[hh]
                kb = k_ref[pl.ds(k0, tk), hh * LANE:(hh + 1) * LANE]
                s = lax.dot_general(qs[hh], kb, (((1,), (1,)), ((), ())), preferred_element_type=F32)
                if diagonal:
                    s = jnp.where(_lower_triangle(tq), s, NEG)
                m_new = jnp.maximum(m, jnp.max(s, axis=-1, keepdims=True))
                p = jnp.exp(s - m_new)
                alpha = jnp.exp(m - m_new)
                l = alpha * l + jnp.sum(p, axis=-1, keepdims=True)
                acc = alpha * acc + jnp.dot(p.astype(MX), vb, preferred_element_type=F32)
                new.append((m_new, l, acc))
            return tuple(new)

        init = tuple((jnp.full((tq, 1), NEG, F32), jnp.zeros((tq, 1), F32), jnp.zeros((tq, LANE), F32)) for _ in range(2))
        carry = lax.fori_loop(0, qi, lambda j, c: block(j, c, False), init)
        (m0, l0, acc0), (m1, l1, acc1) = block(qi, carry, True)
        o_ref[...] = jnp.where(lane < NOPE, acc0 / l0, acc1 / l1).astype(o_ref.dtype)
        lse_ref[...] = jnp.where(lane < NOPE, m0 + jnp.log(l0), m1 + jnp.log(l1))

    return _pcall(body, name=name, grid=(nb, npair, nq),
                  in_specs=[pl.BlockSpec((tq, 2 * LANE), lambda b, p, i: (b * nq + i, p)),
                            pl.BlockSpec((S, 2 * LANE), lambda b, p, i: (b, p)),
                            pl.BlockSpec((S, LANE), lambda b, p, i: (b, p))],
                  out_specs=[pl.BlockSpec((tq, LANE), lambda b, p, i: (b * nq + i, p)),
                             pl.BlockSpec((tq, LANE), lambda b, p, i: (b * nq + i, p))],
                  out_shape=[_sds((T, NH * NOPE), MX), _sds((T, NH * NOPE), F32)], comm=comm)(q, k, v)


def _attn_bwd(q, k, v, o, lse, do, *, nb, name, tq=512, comm=None):
    T = q.shape[0]
    S = T // nb
    tq = _pick(S, tq)
    nq = S // tq
    tk = tq
    npair = NH // 2

    def body(q_ref, k_ref, v_ref, o_ref, lse_ref, do_ref, dq_ref, dk_ref, dv_ref, delta_ref):
        lane = lax.broadcasted_iota(jnp.int32, (tq, LANE), 1)
        first = lane < NOPE
        dq_ref[...] = jnp.zeros_like(dq_ref)

        def delta_step(qi, _):
            q0 = pl.multiple_of(qi * tq, tq)
            prod = do_ref[pl.ds(q0, tq), :] * o_ref[pl.ds(q0, tq), :].astype(F32)
            d0 = jnp.sum(jnp.where(first, prod, 0.0), axis=-1, keepdims=True)
            d1 = jnp.sum(jnp.where(first, 0.0, prod), axis=-1, keepdims=True)
            delta_ref[pl.ds(q0, tq), :] = jnp.where(first, d0, d1)
            return 0

        lax.fori_loop(0, nq, delta_step, 0)

        def kv_step(kj, _):
            k0 = pl.multiple_of(kj * tk, tk)
            kbs = [k_ref[pl.ds(k0, tk), hh * LANE:(hh + 1) * LANE] for hh in range(2)]
            vb = v_ref[pl.ds(k0, tk), :]

            def q_block(qi, carry, diagonal):
                dk0, dk1, dv = carry
                dks = [dk0, dk1]
                q0 = pl.multiple_of(qi * tq, tq)
                dov = do_ref[pl.ds(q0, tq), :]
                lse_v = lse_ref[pl.ds(q0, tq), :]
                delta_v = delta_ref[pl.ds(q0, tq), :]
                for hh in range(2):
                    qh = q_ref[pl.ds(q0, tq), hh * LANE:(hh + 1) * LANE]
                    dob = jnp.where(first if hh == 0 else jnp.logical_not(first), dov, 0.0).astype(MX)
                    s = lax.dot_general(qh, kbs[hh], (((1,), (1,)), ((), ())), preferred_element_type=F32)
                    p = jnp.exp(s - lse_v[:, hh * NOPE:hh * NOPE + 1])
                    if diagonal:
                        p = jnp.where(_lower_triangle(tq), p, 0.0)
                    dp = lax.dot_general(dob, vb, (((1,), (1,)), ((), ())), preferred_element_type=F32)
                    ds = (p * (dp - delta_v[:, hh * NOPE:hh * NOPE + 1])).astype(MX)
                    dks[hh] = dks[hh] + lax.dot_general(ds, qh, (((0,), (0,)), ((), ())), preferred_element_type=F32)
                    dv = dv + lax.dot_general(p.astype(MX), dob, (((0,), (0,)), ((), ())), preferred_element_type=F32)
                    dq_ref[pl.ds(q0, tq), hh * LANE:(hh + 1) * LANE] += jnp.dot(ds, kbs[hh], preferred_element_type=F32)
                return dks[0], dks[1], dv

            zero = jnp.zeros((tk, LANE), F32)
            carry = q_block(kj, (zero, zero, zero), True)
            dk0, dk1, dv = lax.fori_loop(kj + 1, nq, lambda qi, c: q_block(qi, c, False), carry)
            dk_ref[pl.ds(k0, tk), 0:LANE] = dk0
            dk_ref[pl.ds(k0, tk), LANE:2 * LANE] = dk1
            dv_ref[pl.ds(k0, tk), :] = dv
            return 0

        lax.fori_loop(0, nq, kv_step, 0)

    pair256 = pl.BlockSpec((S, 2 * LANE), lambda b, p: (b, p))
    pair128 = pl.BlockSpec((S, LANE), lambda b, p: (b, p))
    return _pcall(body, name=name, grid=(nb, npair),
                  in_specs=[pair256, pair256, pair128, pair128, pair128, pair128],
                  out_specs=[pair256, pair256, pair128],
                  out_shape=[_sds((T, NH * LANE), F32), _sds((T, NH * LANE), F32), _sds((T, NH * NOPE), F32)],
                  scratch=[pltpu.VMEM((S, LANE), F32)], comm=comm)(q, k, v, o, lse, do)


def _run(plan, fn, *args, name, **kw):
    rider = plan.rider(name)
    if rider is None:
        return fn(*args, name=name, **kw)
    outs, landed = fn(*args, name=name, comm=rider, **kw)
    plan.landed(name, landed)
    return outs


def _layer_fwd_bwd(x, tgt, mod3, cos, sin, plan, P):
    nb = mod3.shape[0]
    W = plan.W
    h1, gu1, a1 = _run(plan, _ffn_up_first, x, mod3, P["norm_ffn1"], W["ffn1_in"], sub=0, name="ffn1_up_a")
    gu1, a1 = _run(plan, _ffn_up_second, h1, W["ffn1_in"], gu1, a1, name="ffn1_up_b")
    f1, x1, h2 = _run(plan, _mm_resid_norm, a1, W["ffn1_out"], x, mod3, P["norm_mix"], sub=1, coef=0.5, name="ffn1_out")
    z = _run(plan, _mm, h2, W["w_in"], mode="nt", out_dtype=MX, name="mix_in", tn=1664)
    pooled, mixed, scaled = _pool_fwd(z, W["pool_grp"], P["pool_scale"], nb=nb, name="pool_fwd")
    br_pool = _mm(scaled, W["pool_proj"], mode="nn", out_dtype=MX, name="pool_proj")
    qn, kvn, qp, kvp, q, k, v = _mla_in_fwd(z, W["q_up"], W["kv_up"], P["q_a_norm"], P["kv_a_norm"], cos, sin,
                                            P["q_gain"], P["k_gain"], name="mla_in_fwd")
    attn, lse = _run(plan, _attn_fwd, q, k, v, nb=nb, name="attn_fwd")
    br_mla, merged, mo, x2, h3 = _mix_out_fwd(z, br_pool, attn, W["mla_proj"], W["w_out"], x1, mod3, P["norm_ffn2"], sub=2,
                                              name="mix_out", tm=256)
    gu2, a2 = _ffn_up(h3, W["ffn2_in"], name="ffn2_up")
    dy, df2, st_fin, loss = _mm_loss(a2, W["ffn2_out"], x2, tgt, mod3, name="ffn2_out_loss")

    plan.grad("ffn2_out", _dw(a2, df2, name="d_ffn2_out"))
    dgu2 = _ffn_dact(df2, gu2, W["ffn2_out"], name="ffn2_dact")
    plan.grad("ffn2_in", _ffn_dw_in(h3, dgu2, name="d_ffn2_in"))
    dx2, dmo, st3 = _run(plan, _ffn_dh_norm, dgu2, W["ffn2_in"], x2, dy, mo, mod3, P["norm_ffn2"], sub=2, coef=1.0,
                         name="d_ffn2_h")
    plan.grad("w_out", _dw(merged, dmo, name="d_mix_out"))
    dbr_pool, dbr_mla, dgates, dattn = _mix_out_bwd(dmo, W["w_out"], z, br_pool, br_mla, W["mla_proj"], name="mix_out_bwd", tm=256)
    plan.grad("pool_proj", _dw(scaled, dbr_pool, name="d_pool_proj"))
    dscaled = _mm(dbr_pool, W["pool_proj"], mode="nt", out_dtype=F32, name="d_pool_scaled")
    du_pool, d_pool_grp, d_pool_scale = _pool_bwd(dscaled, mixed, pooled, W["pool_grp"], P["pool_scale"], nb=nb, name="pool_bwd")
    plan.grad("mla_proj", _dw(attn, dbr_mla, name="d_mla_proj"))
    dq, dk, dv = _run(plan, _attn_bwd, q, k, v, attn, lse, dattn, nb=nb, name="attn_bwd")
    dqp, dkvp, dkr, dql, dkvl, st_prep, st_q, st_kv = _mla_in_bwd(
        dq, dk, dv, qp, kvp, z, W["q_up"], W["kv_up"], P["q_a_norm"], P["kv_a_norm"], cos, sin, P["q_gain"], P["k_gain"],
        nb=nb, name="mla_in_bwd")
    plan.grad("q_up", _dw(qn, dqp, name="d_q_up"))
    plan.grad("kv_up", _dw(kvn, dkvp, name="d_kv_up"))
    dz = jnp.concatenate([du_pool, dql, dkvl, dkr, dgates], axis=1)
    plan.grad("w_in", _run(plan, _dw, h2, dz, name="d_mix_in", tm=256, tn=1664, out_t=True))
    dx1, df1, st2 = _run(plan, _mm_norm_bwd, dz, W["w_in"], x1, dx2, f1, mod3, P["norm_mix"], sub=1, coef=0.5, name="d_mix_h")
    plan.grad("ffn1_out", _run(plan, _dw, a1, df1, name="d_ffn1_out"))
    dgu1 = _run(plan, _ffn_dact, df1, gu1, W["ffn1_out"], name="ffn1_dact")
    half = x.shape[1] // 2
    plan.grad("ffn1_in@0", _run(plan, _ffn_dw_in, h1, dgu1, rows=(0, half), name="d_ffn1_in_a"))
    plan.grad("ffn1_in@1", _run(plan, _ffn_dw_in, h1, dgu1, rows=(half, half), name="d_ffn1_in_b"))
    dh1 = _run(plan, _ffn_dh, dgu1, W["ffn1_in"], name="d_ffn1_h")
    grad_x, st1 = _run(plan, _bwd_block, x, dh1, dx1, None, mod3, P["norm_ffn1"], sub=0, coef=0.0, name="bwd_norm1")

    return loss, grad_x, (st1, st2, st3, st_fin, st_q, st_kv, st_prep, d_pool_scale), d_pool_grp


def _padded(rows):
    return rows + -rows % (4 * SUBLANE)


def _pad_rows(a):
    pad = [(0, 0)] * a.ndim
    pad[-2] = (0, _padded(a.shape[-2]) - a.shape[-2])
    return jnp.pad(a, pad)


def _w_in_pieces(shard):
    runs = []
    for ref0, ref1, k0 in ((0, 1152, 0), (1152, 1184, Z_KR + KR_LANE), (1184, N_CHIP * shard, Z_GP)):
        for q in range(N_CHIP):
            lo, hi = max(ref0, q * shard), min(ref1, (q + 1) * shard)
            if lo < hi:
                runs.append((k0 + lo - ref0, q * _padded(shard) + lo - q * shard, hi - lo))
    return runs


def _w_in_stacked_to_kernel(st, shard):
    flat = st.reshape(-1, st.shape[2])
    parts, at = [], 0
    for k0, s0, n in sorted(_w_in_pieces(shard)):
        parts += [jnp.zeros((k0 - at, flat.shape[1]), st.dtype)] * (k0 > at) + [flat[s0:s0 + n]]
        at = k0 + n
    assert at == Z_W
    return jnp.concatenate(parts, axis=0)


def _w_in_kernel_to_stacked(gt, shard):
    parts, at = [], 0
    for s0, k0, n in sorted((s, k, n) for k, s, n in _w_in_pieces(shard)) + [(N_CHIP * _padded(shard), 0, 0)]:
        parts += [jnp.zeros((s0 - at, gt.shape[1]), gt.dtype)] * (s0 > at) + [gt[k0:k0 + n]] * (n > 0)
        at = s0 + n
    return jnp.concatenate(parts, axis=0).reshape(N_CHIP, _padded(shard), gt.shape[1])


def _q_up_to_kernel(w):
    k = w.shape[0]
    return jnp.pad(w.reshape(k, NH, NOPE + ROPE), ((0, 0), (0, 0), (0, LANE - NOPE - ROPE))).reshape(k, NH * LANE)


def _q_up_from_kernel(g):
    k = g.shape[0]
    return g.reshape(k, NH, LANE)[:, :, :NOPE + ROPE].reshape(k, NH * (NOPE + ROPE))


def _kv_up_to_kernel(w):
    k = w.shape[0]
    w3 = w.reshape(k, NH, 2 * NOPE)
    kpart = jnp.pad(w3[:, :, :NOPE], ((0, 0), (0, 0), (0, LANE - NOPE))).reshape(k, NH * LANE)
    return jnp.concatenate([kpart, w3[:, :, NOPE:].reshape(k, NH * NOPE)], axis=1)


def _kv_up_from_kernel(g):
    k = g.shape[0]
    kpart = g[:, :NH * LANE].reshape(k, NH, LANE)[:, :, :NOPE]
    vpart = g[:, NH * LANE:].reshape(k, NH, NOPE)
    return jnp.concatenate([kpart, vpart], axis=2).reshape(k, NH * 2 * NOPE)


def _gain_slab(nope, rope):
    return jnp.concatenate([nope, rope, jnp.zeros((1, LANE - NOPE - ROPE), nope.dtype)], axis=1)


def _rope_tables(positions):
    inv_freq = 10000.0 ** (-jnp.arange(0, ROPE, 2, dtype=F32) / ROPE)
    ang = positions.astype(F32)[:, None] * inv_freq
    ang = jnp.concatenate([ang, ang], axis=-1)
    t = positions.shape[0]
    cos = jnp.concatenate([jnp.ones((t, KR_LANE), F32), jnp.cos(ang), jnp.ones((t, LANE - KR_LANE - ROPE), F32)], axis=1)
    sin = jnp.concatenate([jnp.zeros((t, KR_LANE), F32), jnp.sin(ang), jnp.zeros((t, LANE - KR_LANE - ROPE), F32)], axis=1)
    return cos, sin


def _coords():
    return lax.axis_index("x"), lax.axis_index("y"), lax.axis_index("c")


HBM_SPEC = pl.BlockSpec(memory_space=pl.ANY)
VMEM_SPEC = pl.BlockSpec(memory_space=pltpu.VMEM)


CHIP_RELS = ((1, 0), (0, 1), (1, 1))


def _gather_comm(shards):
    n = len(shards)

    def ici(ins, outs, sems, a, j, x, y, cc):
        half = ins[a].shape[0] // 2
        mine = pl.ds(cc * half, half)
        dx, dy = CHIP_RELS[j]
        return pltpu.make_async_remote_copy(src_ref=ins[a].at[mine], dst_ref=outs[a].at[2 * x + y, mine],
                                            send_sem=sems[0].at[a, j], recv_sem=sems[1].at[a, j],
                                            device_id=(x ^ dx, y ^ dy, cc), device_id_type=MESH)

    def d2d(ins, outs, sems, a, j, x, y, cc, half_of):
        half = ins[a].shape[0] // 2
        dx, dy = CHIP_RELS[j]
        landed = outs[a].at[2 * (x ^ dx) + (y ^ dy), pl.ds(half_of * half, half)]
        return pltpu.make_async_remote_copy(src_ref=landed, dst_ref=landed, send_sem=sems[2].at[a, j], recv_sem=sems[3].at[a, j],
                                            device_id=(x, y, 1 - cc), device_id_type=MESH)

    def own(ins, outs, sems, a, x, y):
        return pltpu.make_async_copy(ins[a], outs[a].at[2 * x + y], sems[4].at[a])

    def start(ins, outs, sems):
        x, y, cc = _coords()
        for a in range(n):
            own(ins, outs, sems, a, x, y).start()
            for j in range(3):
                ici(ins, outs, sems, a, j, x, y, cc).start()

    def finish(ins, outs, sems):
        x, y, cc = _coords()
        for a in range(n):
            for j in range(3):
                ici(ins, outs, sems, a, j, x, y, cc).wait_recv()
                d2d(ins, outs, sems, a, j, x, y, cc, cc).start()
        for a in range(n):
            for j in range(3):
                d2d(ins, outs, sems, a, j, x, y, cc, 1 - cc).wait_recv()
        for a in range(n):
            for j in range(3):
                ici(ins, outs, sems, a, j, x, y, cc).wait_send()
                d2d(ins, outs, sems, a, j, x, y, cc, cc).wait_send()
            own(ins, outs, sems, a, x, y).wait()

    dma = pltpu.SemaphoreType.DMA
    return _Comm(shards, [_sds((N_CHIP,) + s.shape, s.dtype) for s in shards],
                 [dma((n, 3)), dma((n, 3)), dma((n, 3)), dma((n, 3)), dma((n,))], start, finish)


def _swap_comm(parts):
    n = len(parts)

    def copy(ins, outs, sems, a):
        x, y, cc = _coords()
        half = ins[a].shape[1] // 2
        return pltpu.make_async_remote_copy(src_ref=ins[a].at[:, pl.ds((1 - cc) * half, half)], dst_ref=outs[a],
                                            send_sem=sems[0].at[a], recv_sem=sems[1].at[a], device_id=(x, y, 1 - cc),
                                            device_id_type=MESH)

    def start(ins, outs, sems):
        for a in range(n):
            copy(ins, outs, sems, a).start()

    def finish(ins, outs, sems):
        for a in range(n):
            copy(ins, outs, sems, a).wait()

    dma = pltpu.SemaphoreType.DMA
    return _Comm(parts, [_sds((p.shape[0], p.shape[1] // 2, p.shape[2]), p.dtype) for p in parts], [dma((n,)), dma((n,))],
                 start, finish)


def _add_half(full, other, cidx, *, name):
    nch, r, c = full.shape
    half = r // 2
    tr = _pick_rows(half)
    nbk = half // tr
    grid_spec = pltpu.PrefetchScalarGridSpec(
        num_scalar_prefetch=1, grid=(nch, nbk),
        in_specs=[pl.BlockSpec((1, tr, c), lambda j, i, cref: (j, cref[0] * nbk + i, 0)),
                  pl.BlockSpec((1, tr, c), lambda j, i, cref: (j, i, 0))],
        out_specs=pl.BlockSpec((1, tr, c), lambda j, i, cref: (j, i, 0)))

    def body(cref, a_ref, b_ref, o_ref):
        o_ref[...] = (a_ref[...] + b_ref[...]).astype(o_ref.dtype)

    return _pcall(body, name=name, out_shape=_sds((nch, half, c), MX), grid_spec=grid_spec)(cidx, full, other)


def _pick_rows(rows, target=512):
    best = None
    for t in range(16, min(rows, target) + 1, 16):
        if rows % t == 0:
            best = t
    return rows if best is None else best


def _exchange_comm(parts):
    n = len(parts)

    def send(ins, outs, sems, a, j, x, y, cc):
        dx, dy = CHIP_RELS[j]
        return pltpu.make_async_remote_copy(src_ref=ins[a].at[2 * (x ^ dx) + (y ^ dy)], dst_ref=outs[a].at[2 * x + y],
                                            send_sem=sems[0].at[a, j], recv_sem=sems[1].at[a, j],
                                            device_id=(x ^ dx, y ^ dy, cc), device_id_type=MESH)

    def landing(ins, outs, sems, a, j, x, y, cc):
        dx, dy = CHIP_RELS[j]
        peer_chip = 2 * (x ^ dx) + (y ^ dy)
        return pltpu.make_async_remote_copy(src_ref=ins[a].at[peer_chip], dst_ref=outs[a].at[peer_chip], send_sem=sems[0].at[a, j],
                                            recv_sem=sems[1].at[a, j], device_id=(x, y, cc), device_id_type=MESH)

    def own(ins, outs, sems, a, x, y):
        return pltpu.make_async_copy(ins[a].at[2 * x + y], outs[a].at[2 * x + y], sems[2].at[a])

    def start(ins, outs, sems):
        x, y, cc = _coords()
        for a in range(n):
            own(ins, outs, sems, a, x, y).start()
            for j in range(3):
                send(ins, outs, sems, a, j, x, y, cc).start()

    def finish(ins, outs, sems):
        x, y, cc = _coords()
        for a in range(n):
            for j in range(3):
                landing(ins, outs, sems, a, j, x, y, cc).wait_recv()
        for a in range(n):
            for j in range(3):
                send(ins, outs, sems, a, j, x, y, cc).wait_send()
            own(ins, outs, sems, a, x, y).wait()

    dma = pltpu.SemaphoreType.DMA
    return _Comm(parts, [_sds(p.shape, p.dtype) for p in parts], [dma((n, 3)), dma((n, 3)), dma((n,))], start, finish)


def _sum_chips(q, cidx, *, name):
    nch, h, c = q.shape
    tr = _pick_rows(h)
    nbk = h // tr
    grid_spec = pltpu.PrefetchScalarGridSpec(
        num_scalar_prefetch=1, grid=(nbk,),
        in_specs=[pl.BlockSpec((nch, tr, c), lambda i, cref: (0, i, 0))],
        out_specs=pl.BlockSpec((tr, c), lambda i, cref: (cref[0] * nbk + i, 0)))

    def body(cref, q_ref, o_ref):
        acc = q_ref[0].astype(F32) + q_ref[1].astype(F32)
        acc = acc + q_ref[2].astype(F32)
        o_ref[...] = acc + q_ref[3].astype(F32)

    return _pcall(body, name=name, out_shape=_sds((2 * h, c), F32), grid_spec=grid_spec)(cidx, q)


def _join_comm(fulls):
    n = len(fulls)

    def half_copy(outs, sems, a, which):
        x, y, cc = _coords()
        h = outs[a].shape[0] // 2
        rows = outs[a].at[pl.ds((cc if which == 0 else 1 - cc) * h, h)]
        return pltpu.make_async_remote_copy(src_ref=rows, dst_ref=rows, send_sem=sems[0].at[a], recv_sem=sems[1].at[a],
                                            device_id=(x, y, 1 - cc), device_id_type=MESH)

    def start(ins, outs, sems):
        for a in range(n):
            half_copy(outs, sems, a, 0).start()

    def finish(ins, outs, sems):
        for a in range(n):
            half_copy(outs, sems, a, 1).wait_recv()
        for a in range(n):
            half_copy(outs, sems, a, 0).wait_send()

    dma = pltpu.SemaphoreType.DMA
    return _Comm(fulls, [_sds(p.shape, p.dtype) for p in fulls], [dma((n,)), dma((n,))], start, finish,
                 aliases={a: a for a in range(n)})


def _ada_prologue(c, w, b, *, name, comm=None):
    nb, dm = c.shape
    n = w.shape[1]

    def body(c_ref, w_ref, b_ref, call_ref, land_ref, cpad, mod_s, g_send, g_recv, m_send, m_recv):
        x, y, cc = _coords()
        me = 4 * x + 2 * y + cc
        chip = 2 * x + y
        cpad[...] = jnp.zeros_like(cpad)
        cpad[0:nb, :] = c_ref[...]
        copies = []
        for kk in range(1, N_DEV):
            peer = (x ^ (kk >> 2), y ^ ((kk >> 1) & 1), cc ^ (kk & 1))
            cp = pltpu.make_async_remote_copy(src_ref=cpad, dst_ref=call_ref.at[me], send_sem=g_send.at[kk - 1],
                                              recv_sem=g_recv.at[kk - 1], device_id=peer, device_id_type=MESH)
            cp.start()
            copies.append(cp)
        call_ref[me] = cpad[...]
        for kk in range(1, N_DEV):
            pltpu.make_async_remote_copy(src_ref=cpad, dst_ref=call_ref.at[me ^ kk], send_sem=g_send.at[kk - 1],
                                         recv_sem=g_recv.at[kk - 1], device_id=(x, y, cc), device_id_type=MESH).wait_recv()
        cv = call_ref[...].reshape(N_DEV * SUBLANE, dm)
        act = (cv * _sigmoid(cv)).astype(MX)
        mod = jnp.dot(act, w_ref[...].astype(MX), preferred_element_type=F32) + b_ref[...]
        mod_s[...] = mod.reshape(N_DEV, SUBLANE, n)
        for j, (dx, dy) in enumerate(CHIP_RELS):
            cp = pltpu.make_async_remote_copy(src_ref=mod_s.at[4 * (x ^ dx) + 2 * (y ^ dy) + cc], dst_ref=land_ref.at[chip],
                                              send_sem=m_send.at[j], recv_sem=m_recv.at[j],
                                              device_id=(x ^ dx, y ^ dy, cc), device_id_type=MESH)
            cp.start()
            copies.append(cp)
        land_ref[chip] = mod_s[me]
        for j, (dx, dy) in enumerate(CHIP_RELS):
            pltpu.make_async_remote_copy(src_ref=mod_s.at[me], dst_ref=land_ref.at[2 * (x ^ dx) + (y ^ dy)], send_sem=m_send.at[j],
                                         recv_sem=m_recv.at[j], device_id=(x, y, cc), device_id_type=MESH).wait_recv()
        for cp in copies:
            cp.wait_send()

    dma = pltpu.SemaphoreType.DMA
    return _pcall(body, name=name, in_specs=[VMEM_SPEC] * 3, out_specs=[VMEM_SPEC] * 2,
                  out_shape=[_sds((N_DEV, SUBLANE, dm), F32), _sds((N_CHIP, SUBLANE, n), F32)],
                  scratch=[pltpu.VMEM((SUBLANE, dm), F32), pltpu.VMEM((N_DEV, SUBLANE, n), F32),
                           dma((N_DEV - 1,)), dma((N_DEV - 1,)), dma((3,)), dma((3,))], comm=comm)(c, w, b)


def _ada_bwd(c_all, dmod_cols, *, name):
    m, kdim = c_all.shape
    n = dmod_cols.shape[1]
    tn = _pick(n, 1152)

    def body(c_ref, d_ref, o_ref):
        cv = c_ref[...]
        act = (cv * _sigmoid(cv)).astype(MX)
        o_ref[...] = lax.dot_general(act, d_ref[...].astype(MX), (((0,), (0,)), ((), ())), preferred_element_type=F32)

    return _pcall(body, name=name, out_shape=_sds((kdim, n), F32), grid=(n // tn,),
                  in_specs=[pl.BlockSpec((m, kdim), lambda j: (0, 0)), pl.BlockSpec((m, tn), lambda j: (0, j))],
                  out_specs=pl.BlockSpec((kdim, tn), lambda j: (0, j)))(c_all, dmod_cols)


SLAB_W = 1024
RED_ROWS = 8


LOSS_LANE = SLAB_W - LANE


def _pack_stats(st1, st2, st3, st_fin, st_q, st_kv, st_prep, d_pool_scale, loss8, *, name):
    nb = st1.shape[0]
    dm_rows = -(-9 * nb // SUBLANE) * SUBLANE

    def body(s1, s2, s3, sf, sq, skv, sp, sps, loss_ref, o_ref):
        o_ref[...] = jnp.zeros_like(o_ref)
        for s in range(nb):
            rows = [s1[s, 0:1, :], s1[s, 1:2, :], s2[s, 3:4, :], s2[s, 0:1, :], s2[s, 1:2, :], s3[s, 3:4, :], s3[s, 0:1, :],
                    s3[s, 1:2, :], sf[s, 0:1, :]]
            for k, row in enumerate(rows):
                o_ref[9 * s + k:9 * s + k + 1, :] = row

        def over_seq(ref, r):
            acc = ref[0, r:r + 1, :]
            for s in range(1, nb):
                acc = acc + ref[s, r:r + 1, :]
            return acc

        o_ref[dm_rows + 0:dm_rows + 1, :] = over_seq(s1, 2)
        o_ref[dm_rows + 1:dm_rows + 2, :] = over_seq(s2, 2)
        o_ref[dm_rows + 2:dm_rows + 3, :] = over_seq(s3, 2)
        o_ref[dm_rows + 3:dm_rows + 4, 0:POOL_W] = over_seq(sps, 0)
        o_ref[dm_rows + 4:dm_rows + 5, 0:QL] = over_seq(sq, 0)
        o_ref[dm_rows + 5:dm_rows + 6, 0:KVL] = over_seq(skv, 0)
        o_ref[dm_rows + 6:dm_rows + 7, 0:LANE] = over_seq(sp, 0)
        o_ref[dm_rows + 7:dm_rows + 8, 0:LANE] = over_seq(sp, 1)
        o_ref[dm_rows + 7:dm_rows + 8, LOSS_LANE:] = loss_ref[0:1, :]

    return _pcall(body, name=name, out_shape=_sds((dm_rows + RED_ROWS, SLAB_W), F32), in_specs=[VMEM_SPEC] * 9,
                  out_specs=VMEM_SPEC)(st1, st2, st3, st_fin, st_q, st_kv, st_prep, d_pool_scale, loss8)


def _hand_on(a, *, name):
    def body(a_ref, o_ref):
        del a_ref, o_ref

    return _pcall(body, name=name, out_shape=_sds(a.shape, a.dtype), in_specs=[HBM_SPEC], out_specs=HBM_SPEC, aliases={0: 0})(a)


def _small_allreduce(slab, pool, *, name, comm=None):
    rows, w = slab.shape
    dm = rows - RED_ROWS
    prow, pw = pool.shape
    crow = RED_ROWS + 2 * dm

    def body(slab_ref, pool_ref, red_ref, dmod_ref, ptot_ref, sib_slab, sib_pool, chip_slab, chip_pool, land_slab, land_pool,
             a_send, a_recv, b_send, b_recv):
        x, y, cc = _coords()
        chip = 2 * x + y
        sib = (x, y, 1 - cc)
        to_sib = [pltpu.make_async_remote_copy(src_ref=slab_ref, dst_ref=sib_slab, send_sem=a_send.at[0], recv_sem=a_recv.at[0],
                                               device_id=sib, device_id_type=MESH),
                  pltpu.make_async_remote_copy(src_ref=pool_ref, dst_ref=sib_pool, send_sem=a_send.at[1], recv_sem=a_recv.at[1],
                                               device_id=sib, device_id_type=MESH)]
        for cp in to_sib:
            cp.start()
        for cp in to_sib:
            cp.wait()
        mine_dm, theirs_dm = slab_ref[0:dm, :], sib_slab[0:dm, :]
        chip_slab[0:RED_ROWS, :] = slab_ref[dm:, :] + sib_slab[dm:, :]
        chip_slab[RED_ROWS:RED_ROWS + dm, :] = jnp.where(cc == 0, mine_dm, theirs_dm)
        chip_slab[RED_ROWS + dm:, :] = jnp.where(cc == 0, theirs_dm, mine_dm)
        chip_pool[...] = pool_ref[...] + sib_pool[...]

        sends = []
        for j, (dx, dy) in enumerate(CHIP_RELS):
            peer = (x ^ dx, y ^ dy, cc)
            sends.append(pltpu.make_async_remote_copy(src_ref=chip_slab, dst_ref=land_slab.at[chip], send_sem=b_send.at[j, 0],
                                                      recv_sem=b_recv.at[j, 0], device_id=peer, device_id_type=MESH))
            sends.append(pltpu.make_async_remote_copy(src_ref=chip_pool, dst_ref=land_pool.at[chip], send_sem=b_send.at[j, 1],
                                                      recv_sem=b_recv.at[j, 1], device_id=peer, device_id_type=MESH))
        for cp in sends:
            cp.start()
        land_slab[chip] = chip_slab[...]
        land_pool[chip] = chip_pool[...]
        for j, (dx, dy) in enumerate(CHIP_RELS):
            peer_chip = 2 * (x ^ dx) + (y ^ dy)
            pltpu.make_async_remote_copy(src_ref=chip_slab, dst_ref=land_slab.at[peer_chip], send_sem=b_send.at[j, 0],
                                         recv_sem=b_recv.at[j, 0], device_id=(x, y, cc), device_id_type=MESH).wait_recv()
            pltpu.make_async_remote_copy(src_ref=chip_pool, dst_ref=land_pool.at[peer_chip], send_sem=b_send.at[j, 1],
                                         recv_sem=b_recv.at[j, 1], device_id=(x, y, cc), device_id_type=MESH).wait_recv()
        red = land_slab[0, 0:RED_ROWS, :]
        ptot = land_pool[0]
        for ch in range(1, N_CHIP):
            red = red + land_slab[ch, 0:RED_ROWS, :]
            ptot = ptot + land_pool[ch]
        red_ref[...] = red
        ptot_ref[...] = ptot
        for ch in range(N_CHIP):
            dmod_ref[2 * dm * ch:2 * dm * (ch + 1), :] = land_slab[ch, RED_ROWS:, :]
        for cp in sends:
            cp.wait_send()

    dma = pltpu.SemaphoreType.DMA
    return _pcall(body, name=name, in_specs=[VMEM_SPEC, VMEM_SPEC], out_specs=[VMEM_SPEC] * 3,
                  out_shape=[_sds((RED_ROWS, w), F32), _sds((N_DEV * dm, w), F32), _sds((prow, pw), F32)],
                  scratch=[pltpu.VMEM((rows, w), F32), pltpu.VMEM((prow, pw), F32), pltpu.VMEM((crow, w), F32),
                           pltpu.VMEM((prow, pw), F32), pltpu.VMEM((N_CHIP, crow, w), F32), pltpu.VMEM((N_CHIP, prow, pw), F32),
                           dma((2,)), dma((2,)), dma((3, 2)), dma((3, 2))], comm=comm)(slab, pool)


def _adamw_math(w, g, m, v):
    mn = ADAM_B1 * m + (1.0 - ADAM_B1) * g
    vn = ADAM_B2 * v + (1.0 - ADAM_B2) * (g * g)
    bc1 = 1.0 / (1.0 - ADAM_B1 ** ADAM_STEP)
    bc2 = 1.0 / (1.0 - ADAM_B2 ** ADAM_STEP)
    return -ADAM_LR * ((mn * bc1) / (jnp.sqrt(vn * bc2) + ADAM_EPS) + ADAM_WD * w), mn, vn


def _small_update(red, dmod_all, pool_total, nb, params, *, name):
    names = list(SMALL)
    dm = dmod_all.shape[0] // N_DEV

    def grad_of(nm, red_ref, dmod_ref, ptot_ref):
        if nm == "b_ada":
            acc = None
            for d in range(N_DEV):
                for s in range(nb):
                    blk = dmod_ref[d * dm + 9 * s:d * dm + 9 * s + 9, :]
                    acc = blk if acc is None else acc + blk
            return jnp.concatenate([acc[k:k + 1, :] for k in range(9)], axis=1)
        if nm == "pool_grp":
            return ptot_ref[...]
        row, lo, n = {"norm_ffn1": (0, 0, D), "norm_mix": (1, 0, D), "norm_ffn2": (2, 0, D), "pool_scale": (3, 0, POOL_W),
                      "q_a_norm": (4, 0, QL), "kv_a_norm": (5, 0, KVL), "q_norm_nope": (6, 0, NOPE),
                      "q_norm_rope": (6, NOPE, ROPE), "k_norm_nope": (7, 0, NOPE), "k_norm_rope": (7, KR_LANE, ROPE)}[nm]
        return red_ref[row:row + 1, lo:lo + n]

    def body(*refs):
        red_ref, dmod_ref, ptot_ref = refs[:3]
        ins = refs[3:3 + 3 * len(names)]
        outs = refs[3 + 3 * len(names):]
        outs[4 * len(names)][...] = red_ref[RED_ROWS - 1:RED_ROWS, LOSS_LANE:]
        for i, nm in enumerate(names):
            g = grad_of(nm, red_ref, dmod_ref, ptot_ref)
            d, mn, vn = _adamw_math(ins[3 * i][...], g, ins[3 * i + 1][...], ins[3 * i + 2][...])
            outs[4 * i][...] = g
            outs[4 * i + 1][...] = d
            outs[4 * i + 2][...] = mn
            outs[4 * i + 3][...] = vn

    flat_in = [a for nm in names for a in params[nm]]
    out_shape = [_sds(params[nm][0].shape, F32) for nm in names for _ in range(4)] + [_sds((1, LANE), F32)]
    res = _pcall(body, name=name, in_specs=[VMEM_SPEC] * (3 + len(flat_in)), out_specs=[VMEM_SPEC] * len(out_shape),
                 out_shape=out_shape)(red, dmod_all, pool_total, *flat_in)
    return {nm: tuple(res[4 * i:4 * i + 4]) for i, nm in enumerate(names)}, res[-1]


def _adamw(w, g, m, v, *, name, comm=None):
    r, c = w.shape
    tr = _pick_rows(r, 256)
    tc = c if tr < r else _pick(c, 256)
    spec = pl.BlockSpec((tr, tc), lambda i, j: (i, j))
    bc1 = 1.0 / (1.0 - ADAM_B1 ** ADAM_STEP)
    bc2 = 1.0 / (1.0 - ADAM_B2 ** ADAM_STEP)

    def body(w_ref, g_ref, m_ref, v_ref, d_ref, mo_ref, vo_ref):
        gv = g_ref[...]
        mn = ADAM_B1 * m_ref[...] + (1.0 - ADAM_B1) * gv
        vn = ADAM_B2 * v_ref[...] + (1.0 - ADAM_B2) * (gv * gv)
        mo_ref[...] = mn
        vo_ref[...] = vn
        d_ref[...] = -ADAM_LR * ((mn * bc1) / (jnp.sqrt(vn * bc2) + ADAM_EPS) + ADAM_WD * w_ref[...])

    out = _sds((r, c), F32)
    return _pcall(body, name=name, out_shape=[out, out, out], grid=(r // tr, c // tc), in_specs=[spec] * 4, out_specs=[spec] * 3,
                  comm=comm)(w, g, m, v)


BIG = ("w_ffn1_in", "w_ffn1_out", "w_in", "w_pool_proj", "w_q_up", "w_kv_up", "w_mla_proj", "w_out", "w_ffn2_in", "w_ffn2_out")
ROW_SHARDED = ("w_ffn1_out", "w_out", "w_ffn2_out")
KERNEL_NAME = {"w_ffn1_in": "ffn1_in", "w_ffn1_out": "ffn1_out", "w_in": "w_in", "w_pool_proj": "pool_proj", "w_q_up": "q_up",
               "w_kv_up": "kv_up", "w_mla_proj": "mla_proj", "w_out": "w_out", "w_ffn2_in": "ffn2_in", "w_ffn2_out": "ffn2_out"}
WEIGHTS = ("w_ada", "b_ada", "norm_ffn1", "w_ffn1_in", "w_ffn1_out", "norm_mix", "w_in", "pool_grp", "pool_scale", "w_pool_proj",
           "q_a_norm", "w_q_up", "kv_a_norm", "w_kv_up", "q_norm_nope", "q_norm_rope", "k_norm_nope", "k_norm_rope", "w_mla_proj",
           "w_out", "norm_ffn2", "w_ffn2_in", "w_ffn2_out")
SMALL = ("b_ada", "norm_ffn1", "norm_mix", "pool_grp", "pool_scale", "q_a_norm", "kv_a_norm", "q_norm_nope", "q_norm_rope",
         "k_norm_nope", "k_norm_rope", "norm_ffn2")


def _assemble(name, stacked):
    if name in ROW_SHARDED:
        return stacked.reshape(stacked.shape[0] * stacked.shape[1], stacked.shape[2])
    return jnp.transpose(stacked, (1, 0, 2)).reshape(stacked.shape[1], stacked.shape[0] * stacked.shape[2])


def _split(name, full):
    if name in ROW_SHARDED:
        return full.reshape(N_CHIP, full.shape[0] // N_CHIP, full.shape[1])
    return jnp.transpose(full.reshape(full.shape[0], N_CHIP, full.shape[1] // N_CHIP), (1, 0, 2))


GU = ("w_ffn1_in", "w_ffn2_in")
NARROW = ("w_in", "w_q_up")
TRANSPOSED = ("w_in",)
W_IN_SHARD = (Z_W - LANE + ROPE) // N_CHIP


MIX_SMALL = ("w_out", "w_pool_proj", "w_mla_proj", "w_q_up", "w_kv_up")
RIDES = {
    "ada_prologue": (("gather", ("w_ffn1_in",)),),
    "ffn1_up_a": (("gather", ("w_ffn1_out",)),),
    "ffn1_up_b": (("gather", ("w_in",)),),
    "mix_in": (("gather", MIX_SMALL),),
    "attn_fwd": (("gather", ("w_ffn2_in", "w_ffn2_out")),),
    "d_ffn2_h": (("swap", ("w_ffn2_out", "w_ffn2_in")),),
    "attn_bwd": (("exchange", ("w_ffn2_out", "w_ffn2_in")),),
    "d_mix_in": (("swap", MIX_SMALL),),
    "d_mix_h": (("exchange", MIX_SMALL), ("swap", ("w_in",))),
    "ffn1_dact": (("exchange", ("w_in",)), ("swap", ("w_ffn1_out",))),
    "d_ffn1_in_a": (("exchange", ("w_ffn1_out",)),),
    "d_ffn1_in_b": (("swap", ("w_ffn1_in@0",)),),
    "d_ffn1_h": (("exchange", ("w_ffn1_in@0",)), ("swap", ("w_ffn1_in@1",))),
    "bwd_norm1": (("exchange", ("w_ffn1_in@1",)),),
    "small_allreduce": (("join", tuple(n for n in BIG if n != "w_ffn1_in") + ("w_ffn1_in@0", "w_ffn1_in@1")),),
}


def _kname(n):
    base, _, part = n.partition("@")
    return KERNEL_NAME[base] + ("@" + part if part else "")


def _both(comms):
    if len(comms) == 1:
        return comms[0]
    ins, outs, sems, aliases, spans = [], [], [], {}, []
    for c in comms:
        spans.append((len(ins), len(c.ins), len(outs), len(c.out_shapes), len(sems), len(c.sems)))
        aliases.update({len(ins) + i: len(outs) + o for i, o in c.aliases.items()})
        ins, outs, sems = ins + c.ins, outs + c.out_shapes, sems + c.sems

    def each(which):
        def run(i_, o_, s_):
            for c, (ia, ni, oa, no, sa, ns) in zip(comms, spans):
                getattr(c, which)(i_[ia:ia + ni], o_[oa:oa + no], s_[sa:sa + ns])
        return run

    return _Comm(ins, outs, sems, each("start"), each("finish"), aliases)


class _ExchangePlan:
    def __init__(self, shards, cidx):
        self.shards, self.cidx = shards, cidx
        self.W, self.G, self.parts, self.pre, self.reduced, self.joined = {}, {}, {}, {}, {}, {}

    def grad(self, key, g):
        self.G[key] = g

    def rider(self, name):
        comms = []
        for kind, names in RIDES.get(name, ()):
            if kind == "gather":
                comms.append(_gather_comm([self.shards[n] for n in names]))
            elif kind == "swap":
                for n in names:
                    self.parts[n] = self._stacked(n)
                comms.append(_swap_comm([self.parts[n] for n in names]))
            elif kind == "exchange":
                comms.append(_exchange_comm([self.pre[n] for n in names]))
            else:
                comms.append(_join_comm([self.reduced[n] for n in names]))
        return _both(comms) if comms else None

    def landed(self, name, outs):
        at = 0
        for kind, names in RIDES[name]:
            for n, o in zip(names, outs[at:at + len(names)]):
                if kind == "gather":
                    self.W[KERNEL_NAME[n]] = self._to_kernel(n, o)
                elif kind == "swap":
                    self.pre[n] = _add_half(self.parts[n], o, self.cidx, name="rs_add_" + _kname(n))
                elif kind == "exchange":
                    self.reduced[n] = _sum_chips(o, self.cidx, name="rs_sum_" + _kname(n))
                else:
                    self.joined[n] = o
            at += len(names)

    @staticmethod
    def _to_kernel(n, stacked):
        if n in GU:
            return stacked
        if n in TRANSPOSED:
            return _w_in_stacked_to_kernel(stacked, W_IN_SHARD)
        full = _assemble(n, stacked)
        return {"w_q_up": _q_up_to_kernel, "w_kv_up": _kv_up_to_kernel}.get(n, lambda w: w)(full)

    def _stacked(self, n):
        g = self.G[_kname(n)]
        if n.partition("@")[0] in GU:
            return g
        if n in TRANSPOSED:
            return _w_in_kernel_to_stacked(g, W_IN_SHARD)
        full = {"w_q_up": _q_up_from_kernel, "w_kv_up": _kv_up_from_kernel}.get(n, lambda w: w)(g)
        return _split(n, full)


def kernel(x, c, positions, w_ada, b_ada, norm_ffn1, w_ffn1_in, w_ffn1_out, norm_mix, w_in, pool_grp, pool_scale, w_pool_proj, q_a_norm, w_q_up, kv_a_norm, w_kv_up, q_norm_nope, q_norm_rope, k_norm_nope, k_norm_rope, w_mla_proj, w_out, norm_ffn2, w_ffn2_in, w_ffn2_out, loss_target, m_w_ada, m_b_ada, m_norm_ffn1, m_w_ffn1_in, m_w_ffn1_out, m_norm_mix, m_w_in, m_pool_grp, m_pool_scale, m_w_pool_proj, m_q_a_norm, m_w_q_up, m_kv_a_norm, m_w_kv_up, m_q_norm_nope, m_q_norm_rope, m_k_norm_nope, m_k_norm_rope, m_w_mla_proj, m_w_out, m_norm_ffn2, m_w_ffn2_in, m_w_ffn2_out, v_w_ada, v_b_ada, v_norm_ffn1, v_w_ffn1_in, v_w_ffn1_out, v_norm_mix, v_w_in, v_pool_grp, v_pool_scale, v_w_pool_proj, v_q_a_norm, v_w_q_up, v_kv_a_norm, v_w_kv_up, v_q_norm_nope, v_q_norm_rope, v_k_norm_nope, v_k_norm_rope, v_w_mla_proj, v_w_out, v_norm_ffn2, v_w_ffn2_in, v_w_ffn2_out):
    args = dict(locals())
    wts = {n: args[n][0] for n in WEIGHTS}
    mom = {n: args["m_" + n][0] for n in WEIGHTS}
    var = {n: args["v_" + n][0] for n in WEIGHTS}
    nb, seq, dm = x.shape
    tokens = nb * seq
    xi, yi, ci = _coords()
    chip = 2 * xi + yi

    cidx = ci.astype(jnp.int32).reshape(1)
    plan = _ExchangePlan({n: _pad_rows(wts[n].T.astype(MX)) if n in TRANSPOSED else wts[n].astype(MX) for n in BIG}, cidx)
    plan.W["pool_grp"] = wts["pool_grp"].astype(MX)

    ncol = w_ada.shape[2]
    b_cols = lax.dynamic_slice_in_dim(wts["b_ada"].reshape(1, -1), chip * ncol, ncol, axis=1)
    c_slots, mod_slots = _run(plan, _ada_prologue, c, wts["w_ada"], b_cols, name="ada_prologue")
    c_all = c_slots[:, :nb].reshape(N_DEV * nb, dm)
    mod3 = jnp.transpose(mod_slots[:, :nb], (1, 0, 2)).reshape(nb, 9, dm)
    P = {"norm_ffn1": wts["norm_ffn1"].reshape(1, dm), "norm_mix": wts["norm_mix"].reshape(1, dm),
         "norm_ffn2": wts["norm_ffn2"].reshape(1, dm), "pool_scale": wts["pool_scale"].reshape(1, POOL_W),
         "q_a_norm": wts["q_a_norm"].reshape(1, QL), "kv_a_norm": wts["kv_a_norm"].reshape(1, KVL),
         "q_gain": _gain_slab(wts["q_norm_nope"].reshape(1, NOPE), wts["q_norm_rope"].reshape(1, ROPE)),
         "k_gain": _gain_slab(wts["k_norm_nope"].reshape(1, NOPE), wts["k_norm_rope"].reshape(1, ROPE))}
    cos, sin = _rope_tables(positions.reshape(tokens))

    loss8, grad_x, stats, d_pool_grp = _layer_fwd_bwd(x.reshape(tokens, dm), loss_target.reshape(tokens, dm), mod3, cos, sin, plan, P)

    slab = _pack_stats(*stats, loss8, name="pack_stats")
    red, dmod_rows, pool_total = _run(plan, _small_allreduce, slab, d_pool_grp.reshape(POOL_G * LANE, LANE), name="small_allreduce")
    grads_t = {n: plan.joined[n][:W_IN_SHARD] for n in TRANSPOSED}
    grads = {n: grads_t[n].T if n in TRANSPOSED else plan.joined[n] for n in BIG if n != "w_ffn1_in"}
    grads["w_ffn1_in"] = jnp.concatenate([plan.joined["w_ffn1_in@0"], plan.joined["w_ffn1_in@1"]], axis=0)
    dm_rows = dmod_rows.shape[0] // N_DEV
    dmod_all = dmod_rows.reshape(N_DEV, dm_rows, SLAB_W)[:, :9 * nb].reshape(N_DEV * nb, 9 * dm)
    dmod_cols = lax.dynamic_slice_in_dim(dmod_all, chip * ncol, ncol, axis=1)
    grads["w_ada"] = _ada_bwd(c_all, dmod_cols, name="ada_bwd")

    delta, new_m, new_v = {}, {}, {}
    as2d = lambda a: a.reshape(POOL_G * LANE, LANE) if a.ndim == 4 else a.reshape(1, -1)
    upd, loss_row = _small_update(red, dmod_rows, pool_total, nb,
                                  {n: tuple(as2d(args[p + n]) for p in ("", "m_", "v_")) for n in SMALL}, name="small_update")
    loss = loss_row[0, 0]
    for n in SMALL:
        grads[n], delta[n], new_m[n], new_v[n] = upd[n]
    for n in ("w_ada",) + BIG:
        if n in NARROW:
            res = _adamw(wts[n].T, grads_t[n] if n in TRANSPOSED else grads[n].T, mom[n].T, var[n].T, name="adamw_" + n)
            delta[n], new_m[n], new_v[n] = (r.T for r in res)
        else:
            delta[n], new_m[n], new_v[n] = _adamw(wts[n], grads[n], mom[n], var[n], name="adamw_" + n)

    def lead(a, n):
        return a.reshape((1,) + wts[n].shape)

    grad_x = _hand_on(grad_x, name="grad_x")
    return (loss, grad_x.reshape(nb, seq, dm), *[lead(grads[n], n) for n in WEIGHTS], *[lead(delta[n], n) for n in WEIGHTS],
            *[lead(new_m[n], n) for n in WEIGHTS], *[lead(new_v[n], n) for n in WEIGHTS])
```

```python
import functools
import math

import jax
import jax.numpy as jnp
from jax import lax
from jax.experimental import pallas as pl
from jax.experimental.pallas import tpu as pltpu

F32 = jnp.float32
MX = jnp.bfloat16

D = 1024
DFF = 2816
NH = 8
POOL_W = 512
POOL_G = 4
QL = 384
KVL = 256
ROPE = 32
NOPE = 64
LANE = 128
SUBLANE = 8
EPS = 1e-6
ATTN_SCALE = 1.0 / math.sqrt(96.0)
NEG = -1e30

Z_UP, Z_QL, Z_KV, Z_KR, Z_GP, Z_GM, Z_W = 0, 512, 896, 1152, 1280, 2304, 3328
KR_LANE = 64
LAT_W = 1280

ADAM_LR, ADAM_B1, ADAM_B2, ADAM_EPS, ADAM_WD, ADAM_STEP = 0.001, 0.9, 0.999, 1e-08, 0.01, 10

VMEM_LIMIT = 48 * 1024 * 1024
MESH = pl.DeviceIdType.MESH
N_DEV = 8
N_CHIP = 4


class _Comm:
    def __init__(self, ins, out_shapes, sems, start, finish, aliases=None):
        self.ins, self.out_shapes, self.sems = list(ins), list(out_shapes), list(sems)
        self.start, self.finish = start, finish
        self.aliases = aliases or {}


def _pcall(body, *, name, out_shape, grid=(), in_specs=None, out_specs=None, scratch=(), grid_spec=None, aliases=None,
           comm=None):
    params = pltpu.CompilerParams(vmem_limit_bytes=VMEM_LIMIT)
    kw = dict(name=name, compiler_params=params)
    if comm is None:
        if aliases:
            kw["input_output_aliases"] = aliases
        if grid_spec is not None:
            return pl.pallas_call(body, grid_spec=grid_spec, out_shape=out_shape, **kw)
        return pl.pallas_call(body, grid=grid, in_specs=in_specs, out_specs=out_specs, scratch_shapes=scratch,
                              out_shape=out_shape, **kw)
    single = not isinstance(out_shape, (list, tuple))
    outs = [out_shape] if single else list(out_shape)
    ospecs = [out_specs] if single else list(out_specs)
    n_in, n_out, n_ci, n_co, n_scr = len(in_specs), len(outs), len(comm.ins), len(comm.out_shapes), len(scratch)
    io = dict(aliases or {})
    io.update({n_in + i: n_out + o for i, o in comm.aliases.items()})

    def riding(*refs):
        ins, cins = refs[:n_in], refs[n_in:n_in + n_ci]
        at = n_in + n_ci
        os_, couts = refs[at:at + n_out], refs[at + n_out:at + n_out + n_co]
        at += n_out + n_co
        scr, csems = refs[at:at + n_scr], refs[at + n_scr:]
        if grid:
            first = functools.reduce(jnp.logical_and, [pl.program_id(d) == 0 for d in range(len(grid))])
            last = functools.reduce(jnp.logical_and, [pl.program_id(d) == grid[d] - 1 for d in range(len(grid))])
            pl.when(first)(lambda: comm.start(cins, couts, csems))
            body(*ins, *os_, *scr)
            pl.when(last)(lambda: comm.finish(cins, couts, csems))
        else:
            comm.start(cins, couts, csems)
            body(*ins, *os_, *scr)
            comm.finish(cins, couts, csems)

    call = pl.pallas_call(riding, grid=grid, in_specs=list(in_specs) + [HBM_SPEC] * n_ci, out_specs=ospecs + [HBM_SPEC] * n_co,
                          out_shape=outs + comm.out_shapes, scratch_shapes=list(scratch) + comm.sems,
                          input_output_aliases=io, **kw)

    def run(*args):
        res = call(*args, *comm.ins)
        main = list(res[:n_out])
        return (main[0] if single else main), list(res[n_out:])

    return run


def _pick(dim, target):
    best = None
    for t in range(LANE, min(dim, target) + 1, LANE):
        if dim % t == 0:
            best = t
    return dim if best is None else best


def _sds(shape, dtype):
    return jax.ShapeDtypeStruct(shape, dtype)


def _dw(a, g, *, name, tm=512, tn=1024, out_t=False, comm=None):
    return _mm(a, g, mode="tn", out_dtype=F32, name=name, tm=tm, tn=tn, tk=a.shape[0], n_outer=True, out_t=out_t, comm=comm)


def _mm(a, b, *, mode, out_dtype, name, tm=1024, tn=1024, tk=4096, n_outer=False, out_t=False, comm=None):
    if mode == "nn":
        (M, K), (K2, N) = a.shape, b.shape
    elif mode == "nt":
        (M, K), (N, K2) = a.shape, b.shape
    else:
        (K, M), (K2, N) = a.shape, b.shape
    assert K == K2, (name, a.shape, b.shape)
    tm, tn, tk = _pick(M, tm), _pick(N, tn), _pick(K, tk)
    nk = K // tk
    if n_outer:
        ij = lambda g0, g1: (g1, g0)
        grid = (N // tn, M // tm, nk)
    else:
        ij = lambda g0, g1: (g0, g1)
        grid = (M // tm, N // tn, nk)
    if mode == "tn":
        a_spec = pl.BlockSpec((tk, tm), lambda g0, g1, k: (k, ij(g0, g1)[0]))
    else:
        a_spec = pl.BlockSpec((tm, tk), lambda g0, g1, k: (ij(g0, g1)[0], k))
    if mode == "nt":
        b_spec = pl.BlockSpec((tn, tk), lambda g0, g1, k: (ij(g0, g1)[1], k))
    else:
        b_spec = pl.BlockSpec((tk, tn), lambda g0, g1, k: (k, ij(g0, g1)[1]))
    if out_t:
        assert nk == 1, name
        o_spec = pl.BlockSpec((tn, tm), lambda g0, g1, k: ij(g0, g1)[::-1])
    else:
        o_spec = pl.BlockSpec((tm, tn), lambda g0, g1, k: ij(g0, g1))
    dn = {"nn": (((1,), (0,)), ((), ())), "nt": (((1,), (1,)), ((), ())), "tn": (((0,), (0,)), ((), ()))}[mode]

    def dot(a_ref, b_ref):
        return lax.dot_general(a_ref[...].astype(MX), b_ref[...].astype(MX), dn, preferred_element_type=F32)

    def body_one(a_ref, b_ref, o_ref):
        prod = dot(a_ref, b_ref)
        o_ref[...] = (prod.T if out_t else prod).astype(o_ref.dtype)

    def body_acc(a_ref, b_ref, o_ref, acc_ref):
        k = pl.program_id(2)
        part = dot(a_ref, b_ref)

        @pl.when(k == 0)
        def _():
            acc_ref[...] = part

        @pl.when(k > 0)
        def _():
            acc_ref[...] += part

        @pl.when(k == nk - 1)
        def _():
            o_ref[...] = acc_ref[...].astype(o_ref.dtype)

    return _pcall(body_one if nk == 1 else body_acc, name=name, out_shape=_sds((N, M) if out_t else (M, N), out_dtype), grid=grid,
                  in_specs=[a_spec, b_spec], out_specs=o_spec, scratch=[] if nk == 1 else [pltpu.VMEM((tm, tn), F32)],
                  comm=comm)(a, b)


def _gu_shard(q):
    return (q % 2) * 2 + q // 2


def _ffn_up(h, w_st, *, name, tm=512, comm=None):
    T, dm = h.shape
    hw = w_st.shape[2]
    tm = _pick(T, tm)

    def body(h_ref, wg_ref, wu_ref, gu_ref, a_ref):
        hv = h_ref[...]
        g = jnp.dot(hv, wg_ref[0], preferred_element_type=F32)
        u = jnp.dot(hv, wu_ref[0], preferred_element_type=F32)
        gu_ref[:, :hw] = g.astype(gu_ref.dtype)
        gu_ref[:, hw:] = u.astype(gu_ref.dtype)
        a_ref[...] = (g * _sigmoid(g) * u).astype(a_ref.dtype)

    return _pcall(body, name=name, grid=(T // tm, 2),
                  in_specs=[pl.BlockSpec((tm, dm), lambda i, j: (i, 0)), pl.BlockSpec((1, dm, hw), lambda i, j: (j, 0, 0)),
                            pl.BlockSpec((1, dm, hw), lambda i, j: (2 + j, 0, 0))],
                  out_specs=[pl.BlockSpec((tm, 2 * hw), lambda i, j: (i, j)), pl.BlockSpec((tm, hw), lambda i, j: (i, j))],
                  out_shape=[_sds((T, 4 * hw), MX), _sds((T, 2 * hw), MX)], comm=comm)(h, w_st, w_st)


def _ffn_up_first(x, mod3, gain, w_st, *, sub, name, tm=512, comm=None):
    T, dm = x.shape
    hw = w_st.shape[2]
    tm = _pick(T, tm)
    tps = (T // mod3.shape[0]) // tm

    def body(x_ref, mod_ref, n_ref, wg_ref, wu_ref, h_ref, gu_ref, a_ref):
        xv = x_ref[...]
        xn = xv * _rsq(xv) * n_ref[...]
        hv = (xn * (1.0 + mod_ref[0, 3 * sub + 1:3 * sub + 2, :]) + mod_ref[0, 3 * sub:3 * sub + 1, :]).astype(h_ref.dtype)
        h_ref[...] = hv
        g = jnp.dot(hv, wg_ref[0], preferred_element_type=F32)
        u = jnp.dot(hv, wu_ref[0], preferred_element_type=F32)
        gu_ref[:, :hw] = g.astype(gu_ref.dtype)
        gu_ref[:, hw:] = u.astype(gu_ref.dtype)
        a_ref[...] = (g * _sigmoid(g) * u).astype(a_ref.dtype)

    return _pcall(body, name=name, grid=(T // tm,),
                  in_specs=[_row_spec(tm, dm), pl.BlockSpec((1, 9, dm), lambda i: (i // tps, 0, 0)),
                            pl.BlockSpec((1, dm), lambda i: (0, 0)), pl.BlockSpec((1, dm, hw), lambda i: (0, 0, 0)),
                            pl.BlockSpec((1, dm, hw), lambda i: (2, 0, 0))],
                  out_specs=[_row_spec(tm, dm), pl.BlockSpec((tm, 2 * hw), lambda i: (i, 0)), pl.BlockSpec((tm, hw), lambda i: (i, 0))],
                  out_shape=[_sds((T, dm), MX), _sds((T, 4 * hw), MX), _sds((T, 2 * hw), MX)],
                  comm=comm)(x, mod3, gain, w_st, w_st)


def _ffn_up_second(h, w_st, gu, a, *, name, tm=512, comm=None):
    T, dm = h.shape
    hw = w_st.shape[2]
    tm = _pick(T, tm)

    def body(h_ref, wg_ref, wu_ref, gu_in, a_in, gu_ref, a_ref):
        hv = h_ref[...]
        g = jnp.dot(hv, wg_ref[0], preferred_element_type=F32)
        u = jnp.dot(hv, wu_ref[0], preferred_element_type=F32)
        gu_ref[:, :hw] = g.astype(gu_ref.dtype)
        gu_ref[:, hw:] = u.astype(gu_ref.dtype)
        a_ref[...] = (g * _sigmoid(g) * u).astype(a_ref.dtype)

    return _pcall(body, name=name, grid=(T // tm,),
                  in_specs=[_row_spec(tm, dm), pl.BlockSpec((1, dm, hw), lambda i: (1, 0, 0)),
                            pl.BlockSpec((1, dm, hw), lambda i: (3, 0, 0)), HBM_SPEC, HBM_SPEC],
                  out_specs=[pl.BlockSpec((tm, 2 * hw), lambda i: (i, 1)), pl.BlockSpec((tm, hw), lambda i: (i, 1))],
                  out_shape=[_sds(gu.shape, gu.dtype), _sds(a.shape, a.dtype)], aliases={3: 0, 4: 1},
                  comm=comm)(h, w_st, w_st, gu, a)


def _ffn_dact(df, gu, w_out, *, name, tm=512, comm=None):
    T, dm = df.shape
    hw = gu.shape[1] // 4
    tm = _pick(T, tm)

    def body(df_ref, gu_ref, wo_ref, dgu_ref):
        da = lax.dot_general(df_ref[...], wo_ref[...], (((1,), (1,)), ((), ())), preferred_element_type=F32)
        g = gu_ref[:, :hw].astype(F32)
        u = gu_ref[:, hw:].astype(F32)
        s = _sigmoid(g)
        dgu_ref[:, :hw] = (da * u * (s * (1.0 + g * (1.0 - s)))).astype(dgu_ref.dtype)
        dgu_ref[:, hw:] = (da * (g * s)).astype(dgu_ref.dtype)

    return _pcall(body, name=name, grid=(T // tm, 2),
                  in_specs=[pl.BlockSpec((tm, dm), lambda i, j: (i, 0)), pl.BlockSpec((tm, 2 * hw), lambda i, j: (i, j)),
                            pl.BlockSpec((hw, dm), lambda i, j: (j, 0))],
                  out_specs=pl.BlockSpec((tm, 2 * hw), lambda i, j: (i, j)), out_shape=_sds(gu.shape, MX),
                  comm=comm)(df, gu, w_out)


def _ffn_dh(dgu, w_st, *, name, tm=1024, comm=None):
    T = dgu.shape[0]
    _, dm, hw = w_st.shape
    tm = _pick(T, tm)

    def body(d_ref, w_ref, o_ref, acc_ref):
        q = pl.program_id(1)
        part = lax.dot_general(d_ref[...], w_ref[0], (((1,), (1,)), ((), ())), preferred_element_type=F32)

        @pl.when(q == 0)
        def _():
            acc_ref[...] = part

        @pl.when(jnp.logical_and(q > 0, q < 3))
        def _():
            acc_ref[...] += part

        @pl.when(q == 3)
        def _():
            o_ref[...] = acc_ref[...] + part

    return _pcall(body, name=name, grid=(T // tm, 4),
                  in_specs=[pl.BlockSpec((tm, hw), lambda i, q: (i, q)), pl.BlockSpec((1, dm, hw), lambda i, q: (_gu_shard(q), 0, 0))],
                  out_specs=pl.BlockSpec((tm, dm), lambda i, q: (i, 0)), out_shape=_sds((T, dm), F32),
                  scratch=[pltpu.VMEM((tm, dm), F32)], comm=comm)(dgu, w_st)


def _ffn_dw_in(h, dgu, *, name, rows=None, tm=512, comm=None):
    T, dm = h.shape
    hw = dgu.shape[1] // 4
    first, count = rows if rows is not None else (0, dm)
    tm = _pick(count, tm)
    skip = first // tm

    def body(h_ref, d_ref, o_ref):
        o_ref[0] = lax.dot_general(h_ref[...], d_ref[...], (((0,), (0,)), ((), ())), preferred_element_type=F32)

    return _pcall(body, name=name, grid=(4, count // tm),
                  in_specs=[pl.BlockSpec((T, tm), lambda q, i: (0, skip + i)), pl.BlockSpec((T, hw), lambda q, i: (0, q))],
                  out_specs=pl.BlockSpec((1, tm, hw), lambda q, i: (_gu_shard(q), i, 0)),
                  out_shape=_sds((4, count, hw), F32), comm=comm)(h, dgu)


def _rsq(x):
    return lax.rsqrt(jnp.mean(x * x, axis=-1, keepdims=True) + EPS)


def _sigmoid(x):
    return 1.0 / (1.0 + jnp.exp(-x))


def _row_spec(tm, w):
    return pl.BlockSpec((tm, w), lambda i: (i, 0))


def _mm_resid_norm(a, w, x_prev, mod3, gain, *, sub, coef, name, tm=512, comm=None):
    T, k = a.shape
    dm = w.shape[1]
    tm = _pick(T, tm)
    tps = (T // mod3.shape[0]) // tm

    def body(a_ref, w_ref, x_ref, mod_ref, n_ref, f_ref, xo_ref, h_ref):
        f = jnp.dot(a_ref[...], w_ref[...], preferred_element_type=F32)
        f_ref[...] = f
        x = x_ref[...] + coef * mod_ref[0, 3 * sub - 1:3 * sub, :] * f
        xo_ref[...] = x
        xn = x * _rsq(x) * n_ref[...]
        h_ref[...] = (xn * (1.0 + mod_ref[0, 3 * sub + 1:3 * sub + 2, :]) + mod_ref[0, 3 * sub:3 * sub + 1, :]).astype(h_ref.dtype)

    row = _row_spec(tm, dm)
    return _pcall(body, name=name, grid=(T // tm,),
                  in_specs=[_row_spec(tm, k), pl.BlockSpec((k, dm), lambda i: (0, 0)), row,
                            pl.BlockSpec((1, 9, dm), lambda i: (i // tps, 0, 0)), pl.BlockSpec((1, dm), lambda i: (0, 0))],
                  out_specs=[row, row, row], out_shape=[_sds((T, dm), F32), _sds((T, dm), F32), _sds((T, dm), MX)],
                  comm=comm)(a, w, x_prev, mod3, gain)


def _mm_loss(a, w, x2, tgt, mod3, *, name, tm=512):
    T, k = a.shape
    dm = w.shape[1]
    tm = _pick(T, tm)
    tps = (T // mod3.shape[0]) // tm
    mod_spec = pl.BlockSpec((1, 9, dm), lambda i: (i // tps, 0, 0))
    row = _row_spec(tm, dm)
    stat_spec = pl.BlockSpec((1, SUBLANE, dm), lambda i: (i // tps, 0, 0))
    loss_spec = pl.BlockSpec((SUBLANE, LANE), lambda i: (0, 0))

    def body(a_ref, w_ref, x_ref, t_ref, mod_ref, dy_ref, df_ref, st_ref, loss_ref):
        i = pl.program_id(0)
        g = mod_ref[0, 8:9, :]
        f = jnp.dot(a_ref[...], w_ref[...], preferred_element_type=F32)
        err = x_ref[...] + 0.5 * g * f - t_ref[...]
        dy = err * (1.0 / dm)
        dy_ref[...] = dy
        df_ref[...] = (0.5 * g * dy).astype(df_ref.dtype)
        dgate = jnp.sum(0.5 * dy * f, axis=0, keepdims=True)
        part = 0.5 * jnp.sum(jnp.sum(err * err, axis=0, keepdims=True), axis=1, keepdims=True) * (1.0 / dm)

        @pl.when(i % tps == 0)
        def _():
            st_ref[...] = jnp.zeros_like(st_ref)

        @pl.when(i == 0)
        def _():
            loss_ref[...] = jnp.zeros_like(loss_ref)

        st_ref[0, 0:1, :] += dgate
        loss_ref[...] += jnp.broadcast_to(part, loss_ref.shape)

    return _pcall(body, name=name, grid=(T // tm,),
                  in_specs=[_row_spec(tm, k), pl.BlockSpec((k, dm), lambda i: (0, 0)), row, row, mod_spec],
                  out_specs=[row, row, stat_spec, loss_spec],
                  out_shape=[_sds((T, dm), F32), _sds((T, dm), MX), _sds((mod3.shape[0], SUBLANE, dm), F32),
                             _sds((SUBLANE, LANE), F32)])(a, w, x2, tgt, mod3)


def _norm_bwd_tail(dhv, x_ref, dxi_ref, f_ref, mod_ref, n_ref, dx_ref, df_ref, st_ref, *, first, sub, coef):
    x = x_ref[...]
    r = _rsq(x)
    xhat = x * r
    n = n_ref[...]
    d_shift = jnp.sum(dhv, axis=0, keepdims=True)
    d_scale = jnp.sum(dhv * (xhat * n), axis=0, keepdims=True)
    dxn = dhv * (1.0 + mod_ref[0, 3 * sub + 1:3 * sub + 2, :])
    d_gain = jnp.sum(dxn * xhat, axis=0, keepdims=True)
    dxhat = dxn * n
    dx = dxi_ref[...] + r * (dxhat - xhat * jnp.mean(dxhat * xhat, axis=-1, keepdims=True))
    dx_ref[...] = dx

    @pl.when(first)
    def _():
        st_ref[...] = jnp.zeros_like(st_ref)

    st_ref[0, 0:1, :] += d_shift
    st_ref[0, 1:2, :] += d_scale
    st_ref[0, 2:3, :] += d_gain
    if f_ref is not None:
        st_ref[0, 3:4, :] += jnp.sum(coef * dx * f_ref[...], axis=0, keepdims=True)
        df_ref[...] = (coef * mod_ref[0, 3 * sub - 1:3 * sub, :] * dx).astype(df_ref.dtype)


def _ffn_dh_norm(dgu, w_st, x_cur, dx_in, f_prev, mod3, gain, *, sub, coef, name, tm=512, comm=None):
    T = dgu.shape[0]
    _, dm, hw = w_st.shape
    tm = _pick(T, tm)
    nb = mod3.shape[0]
    tps = (T // nb) // tm

    def body(d_ref, w_ref, x_ref, dxi_ref, f_ref, mod_ref, n_ref, dx_ref, df_ref, st_ref, acc_ref):
        i = pl.program_id(0)
        q = pl.program_id(1)
        part = lax.dot_general(d_ref[...], w_ref[0], (((1,), (1,)), ((), ())), preferred_element_type=F32)

        @pl.when(q == 0)
        def _():
            acc_ref[...] = part

        @pl.when(q > 0)
        def _():
            acc_ref[...] += part

        @pl.when(q == 3)
        def _():
            _norm_bwd_tail(acc_ref[...], x_ref, dxi_ref, f_ref, mod_ref, n_ref, dx_ref, df_ref, st_ref,
                           first=i % tps == 0, sub=sub, coef=coef)

    row = pl.BlockSpec((tm, dm), lambda i, q: (i, 0))
    return _pcall(body, name=name, grid=(T // tm, 4),
                  in_specs=[pl.BlockSpec((tm, hw), lambda i, q: (i, q)), pl.BlockSpec((1, dm, hw), lambda i, q: (_gu_shard(q), 0, 0)),
                            row, row, row, pl.BlockSpec((1, 9, dm), lambda i, q: (i // tps, 0, 0)),
                            pl.BlockSpec((1, dm), lambda i, q: (0, 0))],
                  out_specs=[row, row, pl.BlockSpec((1, SUBLANE, dm), lambda i, q: (i // tps, 0, 0))],
                  out_shape=[_sds((T, dm), F32), _sds((T, dm), MX), _sds((nb, SUBLANE, dm), F32)],
                  scratch=[pltpu.VMEM((tm, dm), F32)], comm=comm)(dgu, w_st, x_cur, dx_in, f_prev, mod3, gain)


def _mm_norm_bwd(a, b, x_cur, dx_in, f_prev, mod3, gain, *, sub, coef, name, tm=512, comm=None):
    T, k = a.shape
    dm = b.shape[1]
    tm = _pick(T, tm)
    nb = mod3.shape[0]
    tps = (T // nb) // tm

    def body(a_ref, b_ref, x_ref, dxi_ref, f_ref, mod_ref, n_ref, dx_ref, df_ref, st_ref):
        dh = jnp.dot(a_ref[...], b_ref[...], preferred_element_type=F32)
        _norm_bwd_tail(dh, x_ref, dxi_ref, f_ref, mod_ref, n_ref, dx_ref, df_ref, st_ref,
                       first=pl.program_id(0) % tps == 0, sub=sub, coef=coef)

    row = _row_spec(tm, dm)
    return _pcall(body, name=name, grid=(T // tm,),
                  in_specs=[_row_spec(tm, k), pl.BlockSpec((k, dm), lambda i: (0, 0)), row, row, row,
                            pl.BlockSpec((1, 9, dm), lambda i: (i // tps, 0, 0)), pl.BlockSpec((1, dm), lambda i: (0, 0))],
                  out_specs=[row, row, pl.BlockSpec((1, SUBLANE, dm), lambda i: (i // tps, 0, 0))],
                  out_shape=[_sds((T, dm), F32), _sds((T, dm), MX), _sds((nb, SUBLANE, dm), F32)],
                  comm=comm)(a, b, x_cur, dx_in, f_prev, mod3, gain)


def _bwd_block(x_cur, dh, dx_in, f_prev, mod3, gain, *, sub, coef, name, tm=256, comm=None):
    T, dm = x_cur.shape
    nb = mod3.shape[0]
    tps = (T // nb) // tm
    has_f = f_prev is not None
    mod_spec = pl.BlockSpec((1, 9, dm), lambda i: (i // tps, 0, 0))
    vec_spec = pl.BlockSpec((1, dm), lambda i: (0, 0))
    stat_spec = pl.BlockSpec((1, SUBLANE, dm), lambda i: (i // tps, 0, 0))
    row = _row_spec(tm, dm)

    def body(*refs):
        if has_f:
            x_ref, dh_ref, dxi_ref, f_ref, mod_ref, n_ref, dx_ref, df_ref, st_ref = refs
        else:
            x_ref, dh_ref, dxi_ref, mod_ref, n_ref, dx_ref, st_ref = refs
            f_ref = df_ref = None
        _norm_bwd_tail(dh_ref[...], x_ref, dxi_ref, f_ref, mod_ref, n_ref, dx_ref, df_ref, st_ref,
                       first=pl.program_id(0) % tps == 0, sub=sub, coef=coef)

    st_shape = _sds((nb, SUBLANE, dm), F32)
    if has_f:
        return _pcall(body, name=name, grid=(T // tm,), in_specs=[row, row, row, row, mod_spec, vec_spec],
                      out_specs=[row, row, stat_spec],
                      out_shape=[_sds((T, dm), F32), _sds((T, dm), MX), st_shape], comm=comm)(x_cur, dh, dx_in, f_prev, mod3, gain)
    seq_row = pl.BlockSpec((None, tm, dm), lambda i: (i // tps, i % tps, 0))
    return _pcall(body, name=name, grid=(T // tm,), in_specs=[row, row, row, mod_spec, vec_spec],
                  out_specs=[seq_row, stat_spec], out_shape=[_sds((nb, T // nb, dm), F32), st_shape],
                  comm=comm)(x_cur, dh, dx_in, mod3, gain)


def _mix_out_fwd(z, br_pool, attn, w_mla, w_out, x_prev, mod3, gain, *, sub, name, tm=512):
    T = z.shape[0]
    dm = w_out.shape[1]
    tm = _pick(T, tm)
    tps = (T // mod3.shape[0]) // tm
    whole = lambda a: pl.BlockSpec(a.shape, lambda i: (0, 0))

    def body(z_ref, bp_ref, at_ref, wm_ref, wo_ref, x_ref, mod_ref, n_ref, bm_ref, mg_ref, mo_ref, xo_ref, h_ref):
        bm = jnp.dot(at_ref[...], wm_ref[...], preferred_element_type=F32).astype(MX)
        bm_ref[...] = bm
        gp = z_ref[:, Z_GP:Z_GP + D].astype(F32)
        gm = z_ref[:, Z_GM:Z_GM + D].astype(F32)
        merged = (_sigmoid(gp) * bp_ref[...] + _sigmoid(gm) * bm).astype(MX)
        mg_ref[...] = merged
        mo = jnp.dot(merged, wo_ref[...], preferred_element_type=F32)
        mo_ref[...] = mo
        x = x_ref[...] + mod_ref[0, 3 * sub - 1:3 * sub, :] * mo
        xo_ref[...] = x
        xn = x * _rsq(x) * n_ref[...]
        h_ref[...] = (xn * (1.0 + mod_ref[0, 3 * sub + 1:3 * sub + 2, :]) + mod_ref[0, 3 * sub:3 * sub + 1, :]).astype(h_ref.dtype)

    row = _row_spec(tm, dm)
    return _pcall(body, name=name, grid=(T // tm,),
                  in_specs=[_row_spec(tm, Z_W), row, _row_spec(tm, attn.shape[1]), whole(w_mla), whole(w_out), row,
                            pl.BlockSpec((1, 9, dm), lambda i: (i // tps, 0, 0)), pl.BlockSpec((1, dm), lambda i: (0, 0))],
                  out_specs=[row, row, row, row, row],
                  out_shape=[_sds((T, dm), MX), _sds((T, dm), MX), _sds((T, dm), F32), _sds((T, dm), F32), _sds((T, dm), MX)],
                  )(z, br_pool, attn, w_mla, w_out, x_prev, mod3, gain)


def _mix_out_bwd(dmo, w_out, z, br_pool, br_mla, w_mla, *, name, tm=512):
    T = z.shape[0]
    tm = _pick(T, tm)
    whole = lambda a: pl.BlockSpec(a.shape, lambda i: (0, 0))
    nt = (((1,), (1,)), ((), ()))

    def body(dmo_ref, wo_ref, z_ref, bp_ref, bm_ref, wm_ref, dbp_ref, dbm_ref, dg_ref, dat_ref):
        dm = lax.dot_general(dmo_ref[...], wo_ref[...], nt, preferred_element_type=F32)
        sp = _sigmoid(z_ref[:, Z_GP:Z_GP + D].astype(F32))
        sm = _sigmoid(z_ref[:, Z_GM:Z_GM + D].astype(F32))
        dbp_ref[...] = (dm * sp).astype(dbp_ref.dtype)
        dbm = (dm * sm).astype(MX)
        dbm_ref[...] = dbm
        dg_ref[:, :D] = (dm * bp_ref[...].astype(F32) * sp * (1.0 - sp)).astype(dg_ref.dtype)
        dg_ref[:, D:] = (dm * bm_ref[...].astype(F32) * sm * (1.0 - sm)).astype(dg_ref.dtype)
        dat_ref[...] = lax.dot_general(dbm, wm_ref[...], nt, preferred_element_type=F32)

    row = _row_spec(tm, D)
    return _pcall(body, name=name, grid=(T // tm,),
                  in_specs=[row, whole(w_out), _row_spec(tm, Z_W), row, row, whole(w_mla)],
                  out_specs=[row, row, _row_spec(tm, 2 * D), _row_spec(tm, w_mla.shape[0])],
                  out_shape=[_sds((T, D), MX), _sds((T, D), MX), _sds((T, 2 * D), MX), _sds((T, w_mla.shape[0]), F32)],
                  )(dmo, w_out, z, br_pool, br_mla, w_mla)


def _shift_down(x, k, row):
    return jnp.where(row >= k, pltpu.roll(x, k, 0), 0.0)


def _shift_up(x, k, row, n):
    return jnp.where(row < n - k, pltpu.roll(x, n - k, 0), 0.0)


def _pool_fwd(z, pool_grp, pool_scale, *, nb, name):
    T = z.shape[0]
    S = T // nb
    blk = pl.BlockSpec((S, LANE), lambda b, g: (b, g))

    def body(u_ref, w_ref, s_ref, pooled_ref, mixed_ref, scaled_ref):
        g = pl.program_id(1)
        u = u_ref[...].astype(F32)
        row = lax.broadcasted_iota(jnp.int32, u.shape, 0)
        s2 = u + _shift_down(u, 1, row)
        s4 = s2 + _shift_down(s2, 2, row)
        s8 = s4 + _shift_down(s4, 4, row)
        s16 = s8 + _shift_down(s8, 8, row)
        win = jnp.where(g == 0, s2, jnp.where(g == 1, s4, jnp.where(g == 2, s8, s16)))
        width = lax.shift_left(jnp.int32(2), g)
        cnt = jnp.minimum(row + 1, width).astype(F32)
        pooled = (win / cnt - u).astype(MX)
        pooled_ref[...] = pooled
        mixed = jnp.dot(pooled, w_ref[0], preferred_element_type=F32)
        mixed_ref[...] = mixed
        scaled_ref[...] = (mixed * s_ref[...]).astype(scaled_ref.dtype)

    return _pcall(body, name=name, grid=(nb, POOL_G),
                  in_specs=[blk, pl.BlockSpec((1, LANE, LANE), lambda b, g: (g, 0, 0)),
                            pl.BlockSpec((1, LANE), lambda b, g: (0, g))],
                  out_specs=[blk, blk, blk],
                  out_shape=[_sds((T, POOL_W), MX), _sds((T, POOL_W), F32), _sds((T, POOL_W), MX)])(z, pool_grp, pool_scale)


def _pool_bwd(dscaled, mixed, pooled, pool_grp, pool_scale, *, nb, name):
    T = dscaled.shape[0]
    S = T // nb
    blk = pl.BlockSpec((S, LANE), lambda g, b: (b, g))

    def body(ds_ref, mixed_ref, pooled_ref, w_ref, s_ref, du_ref, dw_ref, dsc_ref):
        g = pl.program_id(0)
        b = pl.program_id(1)
        ds = ds_ref[...]
        dsc_ref[0] = jnp.sum(ds * mixed_ref[...], axis=0, keepdims=True)
        dmixed = (ds * s_ref[...]).astype(MX)
        dw = lax.dot_general(pooled_ref[...], dmixed, (((0,), (0,)), ((), ())), preferred_element_type=F32)

        @pl.when(b == 0)
        def _():
            dw_ref[0] = dw

        @pl.when(b > 0)
        def _():
            dw_ref[0] += dw
        dpooled = lax.dot_general(dmixed, w_ref[0], (((1,), (1,)), ((), ())), preferred_element_type=F32)
        row = lax.broadcasted_iota(jnp.int32, dpooled.shape, 0)
        width = lax.shift_left(jnp.int32(2), g)
        q = dpooled / jnp.minimum(row + 1, width).astype(F32)
        r2 = q + _shift_up(q, 1, row, S)
        r4 = r2 + _shift_up(r2, 2, row, S)
        r8 = r4 + _shift_up(r4, 4, row, S)
        r16 = r8 + _shift_up(r8, 8, row, S)
        win = jnp.where(g == 0, r2, jnp.where(g == 1, r4, jnp.where(g == 2, r8, r16)))
        du_ref[...] = (win - dpooled).astype(du_ref.dtype)

    return _pcall(body, name=name, grid=(POOL_G, nb),
                  in_specs=[blk, blk, blk, pl.BlockSpec((1, LANE, LANE), lambda g, b: (g, 0, 0)),
                            pl.BlockSpec((1, LANE), lambda g, b: (0, g))],
                  out_specs=[blk, pl.BlockSpec((1, LANE, LANE), lambda g, b: (g, 0, 0)),
                             pl.BlockSpec((1, 1, LANE), lambda g, b: (b, 0, g))],
                  out_shape=[_sds((T, POOL_W), MX), _sds((POOL_G, LANE, LANE), F32), _sds((nb, 1, POOL_W), F32)],
                  )(dscaled, mixed, pooled, pool_grp, pool_scale)


def _lane_masks(shape):
    lane = lax.broadcasted_iota(jnp.int32, shape, len(shape) - 1)
    m_n = lane < NOPE
    m_r = jnp.logical_and(lane >= KR_LANE, lane < KR_LANE + ROPE)
    first_half = lane < KR_LANE + ROPE // 2
    return m_n, m_r, first_half


def _rot(y, first_half):
    return jnp.where(first_half, -pltpu.roll(y, LANE - ROPE // 2, 1), pltpu.roll(y, ROPE // 2, 1))


def _rot_t(v, first_half, m_r):
    return jnp.where(m_r, jnp.where(first_half, pltpu.roll(v, LANE - ROPE // 2, 1), -pltpu.roll(v, ROPE // 2, 1)), 0.0)


def _mla_in_fwd(z, w_q, w_kv, qa_gain, kva_gain, cos, sin, q_gain, k_gain, *, name, tm=256):
    T = z.shape[0]
    slab = pl.BlockSpec((tm, LANE), lambda i: (i, 0))
    vec = pl.BlockSpec((1, LANE), lambda i: (0, 0))
    whole = lambda a: pl.BlockSpec(a.shape, lambda i: (0, 0))

    def body(z_ref, wq_ref, wkv_ref, qa_ref, kva_ref, cos_ref, sin_ref, qg_ref, kg_ref,
             qn_ref, kvn_ref, qp_ref, kvp_ref, q_ref, k_ref, v_ref):
        ql = z_ref[:, Z_QL:Z_QL + QL].astype(F32)
        kvl = z_ref[:, Z_KV:Z_KV + KVL].astype(F32)
        qn = (ql * _rsq(ql) * qa_ref[...]).astype(MX)
        kvn = (kvl * _rsq(kvl) * kva_ref[...]).astype(MX)
        qn_ref[...] = qn
        kvn_ref[...] = kvn
        qp = jnp.dot(qn, wq_ref[...], preferred_element_type=F32)
        kvp = jnp.dot(kvn, wkv_ref[...], preferred_element_type=F32)
        qp_ref[...] = qp
        kvp_ref[...] = kvp
        m_n, m_r, first_half = _lane_masks((tm, LANE))
        c = cos_ref[...]
        s = sin_ref[...]
        qg = qg_ref[...]
        kg = kg_ref[...]
        xr = z_ref[:, Z_KR:Z_KR + LANE].astype(F32)
        rr = lax.rsqrt(jnp.sum(xr * xr, axis=-1, keepdims=True) * (1.0 / ROPE) + EPS)
        yr = xr * rr * kg
        kr = jnp.where(m_r, yr * c + _rot(yr, first_half) * s, 0.0)
        for h in range(NH):
            x = qp[:, h * LANE:(h + 1) * LANE]
            x2 = x * x
            rn = lax.rsqrt(jnp.sum(jnp.where(m_n, x2, 0.0), axis=-1, keepdims=True) * (1.0 / NOPE) + EPS)
            rq = lax.rsqrt(jnp.sum(jnp.where(m_r, x2, 0.0), axis=-1, keepdims=True) * (1.0 / ROPE) + EPS)
            y = x * jnp.where(m_n, rn, jnp.where(m_r, rq, 0.0)) * qg
            q_ref[:, h * LANE:(h + 1) * LANE] = ((y * c + _rot(y, first_half) * s) * ATTN_SCALE).astype(q_ref.dtype)
            xk = kvp[:, h * LANE:(h + 1) * LANE]
            rk = lax.rsqrt(jnp.sum(jnp.where(m_n, xk * xk, 0.0), axis=-1, keepdims=True) * (1.0 / NOPE) + EPS)
            k_ref[:, h * LANE:(h + 1) * LANE] = (jnp.where(m_n, xk * rk * kg, 0.0) + kr).astype(k_ref.dtype)
        v_ref[...] = kvp[:, NH * LANE:].astype(v_ref.dtype)

    return _pcall(body, name=name, grid=(T // tm,),
                  in_specs=[_row_spec(tm, LAT_W), whole(w_q), whole(w_kv), whole(qa_gain), whole(kva_gain), slab, slab, vec, vec],
                  out_specs=[_row_spec(tm, QL), _row_spec(tm, KVL), _row_spec(tm, NH * LANE), _row_spec(tm, NH * LANE + NH * NOPE),
                             _row_spec(tm, NH * LANE), _row_spec(tm, NH * LANE), _row_spec(tm, NH * NOPE)],
                  out_shape=[_sds((T, QL), MX), _sds((T, KVL), MX), _sds((T, NH * LANE), F32), _sds((T, NH * LANE + NH * NOPE), F32),
                             _sds((T, NH * LANE), MX), _sds((T, NH * LANE), MX), _sds((T, NH * NOPE), MX)],
                  )(z, w_q, w_kv, qa_gain, kva_gain, cos, sin, q_gain, k_gain)


def _mla_in_bwd(dq, dk, dv, qp, kvp, z, w_q, w_kv, qa_gain, kva_gain, cos, sin, q_gain, k_gain, *, nb, name, tm=256):
    T = dq.shape[0]
    tps = (T // nb) // tm
    slab = pl.BlockSpec((tm, LANE), lambda i: (i, 0))
    vec = pl.BlockSpec((1, LANE), lambda i: (0, 0))
    whole = lambda a: pl.BlockSpec(a.shape, lambda i: (0, 0))

    def latent_bwd(x, dy, gain):
        r = _rsq(x)
        xhat = x * r
        dxhat = dy * gain
        return r * (dxhat - xhat * jnp.mean(dxhat * xhat, axis=-1, keepdims=True)), jnp.sum(dy * xhat, axis=0, keepdims=True)

    def body(dq_ref, dk_ref, dv_ref, qp_ref, kvp_ref, z_ref, wq_ref, wkv_ref, qa_ref, kva_ref, cos_ref, sin_ref, qg_ref, kg_ref,
             dqp_ref, dkvp_ref, dkr_ref, dql_ref, dkvl_ref, st_ref, sq_ref, sk_ref):
        i = pl.program_id(0)
        qp, kvp = qp_ref, kvp_ref
        m_n, m_r, first_half = _lane_masks((tm, LANE))
        c = cos_ref[...]
        s = sin_ref[...]
        qg = qg_ref[...]
        kg = kg_ref[...]
        dqg = jnp.zeros((1, LANE), F32)
        dkg = jnp.zeros((1, LANE), F32)
        dkr_sum = jnp.zeros((tm, LANE), F32)
        for h in range(NH):
            x = qp[:, h * LANE:(h + 1) * LANE]
            x2 = x * x
            rn = lax.rsqrt(jnp.sum(jnp.where(m_n, x2, 0.0), axis=-1, keepdims=True) * (1.0 / NOPE) + EPS)
            rq = lax.rsqrt(jnp.sum(jnp.where(m_r, x2, 0.0), axis=-1, keepdims=True) * (1.0 / ROPE) + EPS)
            rfac = jnp.where(m_n, rn, jnp.where(m_r, rq, 0.0))
            xhat = x * rfac
            do = dq_ref[:, h * LANE:(h + 1) * LANE] * ATTN_SCALE
            dy = do * c + _rot_t(do * s, first_half, m_r)
            dqg = dqg + jnp.sum(dy * xhat, axis=0, keepdims=True)
            dxhat = dy * qg
            t = dxhat * xhat
            mean_n = jnp.sum(jnp.where(m_n, t, 0.0), axis=-1, keepdims=True) * (1.0 / NOPE)
            mean_r = jnp.sum(jnp.where(m_r, t, 0.0), axis=-1, keepdims=True) * (1.0 / ROPE)
            dqp_ref[:, h * LANE:(h + 1) * LANE] = (
                rfac * (dxhat - xhat * jnp.where(m_n, mean_n, jnp.where(m_r, mean_r, 0.0)))).astype(dqp_ref.dtype)

            xk = kvp[:, h * LANE:(h + 1) * LANE]
            rk = lax.rsqrt(jnp.sum(jnp.where(m_n, xk * xk, 0.0), axis=-1, keepdims=True) * (1.0 / NOPE) + EPS)
            khat = jnp.where(m_n, xk * rk, 0.0)
            dko = dk_ref[:, h * LANE:(h + 1) * LANE]
            dkn = jnp.where(m_n, dko, 0.0)
            dkg = dkg + jnp.sum(dkn * khat, axis=0, keepdims=True)
            dkhat = dkn * kg
            mean_k = jnp.sum(dkhat * khat, axis=-1, keepdims=True) * (1.0 / NOPE)
            dkvp_ref[:, h * LANE:(h + 1) * LANE] = jnp.where(m_n, rk * (dkhat - khat * mean_k), 0.0).astype(dkvp_ref.dtype)
            dkr_sum = dkr_sum + jnp.where(m_r, dko, 0.0)
        dkvp_ref[:, NH * LANE:] = dv_ref[...].astype(dkvp_ref.dtype)

        xr = z_ref[:, Z_KR:Z_KR + LANE].astype(F32)
        rr = lax.rsqrt(jnp.sum(xr * xr, axis=-1, keepdims=True) * (1.0 / ROPE) + EPS)
        rhat = xr * rr
        dyr = dkr_sum * c + _rot_t(dkr_sum * s, first_half, m_r)
        dkg = dkg + jnp.sum(dyr * rhat, axis=0, keepdims=True)
        drhat = dyr * kg
        mean_kr = jnp.sum(drhat * rhat, axis=-1, keepdims=True) * (1.0 / ROPE)
        dkr_ref[...] = jnp.where(m_r, rr * (drhat - rhat * mean_kr), 0.0).astype(dkr_ref.dtype)

        nt = (((1,), (1,)), ((), ()))
        dqn = lax.dot_general(dqp_ref[...], wq_ref[...], nt, preferred_element_type=F32)
        dkvn = lax.dot_general(dkvp_ref[...], wkv_ref[...], nt, preferred_element_type=F32)
        dql, dqa = latent_bwd(z_ref[:, Z_QL:Z_QL + QL].astype(F32), dqn, qa_ref[...])
        dkvl, dkva = latent_bwd(z_ref[:, Z_KV:Z_KV + KVL].astype(F32), dkvn, kva_ref[...])
        dql_ref[...] = dql.astype(dql_ref.dtype)
        dkvl_ref[...] = dkvl.astype(dkvl_ref.dtype)

        @pl.when(i % tps == 0)
        def _():
            st_ref[...] = jnp.zeros_like(st_ref)
            sq_ref[...] = jnp.zeros_like(sq_ref)
            sk_ref[...] = jnp.zeros_like(sk_ref)

        st_ref[0, 0:1, :] += dqg
        st_ref[0, 1:2, :] += dkg
        sq_ref[0, 0:1, :] += dqa
        sk_ref[0, 0:1, :] += dkva

    stat = lambda w: pl.BlockSpec((1, SUBLANE, w), lambda i: (i // tps, 0, 0))
    return _pcall(body, name=name, grid=(T // tm,),
                  in_specs=[_row_spec(tm, NH * LANE), _row_spec(tm, NH * LANE), _row_spec(tm, NH * NOPE),
                            _row_spec(tm, NH * LANE), _row_spec(tm, NH * LANE + NH * NOPE), _row_spec(tm, LAT_W),
                            whole(w_q), whole(w_kv), whole(qa_gain), whole(kva_gain), slab, slab, vec, vec],
                  out_specs=[_row_spec(tm, NH * LANE), _row_spec(tm, NH * LANE + NH * NOPE), slab, _row_spec(tm, QL),
                             _row_spec(tm, KVL), stat(LANE), stat(QL), stat(KVL)],
                  out_shape=[_sds((T, NH * LANE), MX), _sds((T, NH * LANE + NH * NOPE), MX), _sds((T, LANE), MX),
                             _sds((T, QL), MX), _sds((T, KVL), MX), _sds((nb, SUBLANE, LANE), F32),
                             _sds((nb, SUBLANE, QL), F32), _sds((nb, SUBLANE, KVL), F32)],
                  )(dq, dk, dv, qp, kvp, z, w_q, w_kv, qa_gain, kva_gain, cos, sin, q_gain, k_gain)


def _lower_triangle(t):
    return lax.broadcasted_iota(jnp.int32, (t, t), 1) <= lax.broadcasted_iota(jnp.int32, (t, t), 0)


def _attn_fwd(q, k, v, *, nb, name, tq=512, comm=None):
    T = q.shape[0]
    S = T // nb
    tq = _pick(S, tq)
    nq = S // tq
    tk = tq
    npair = NH // 2

    def body(q_ref, k_ref, v_ref, o_ref, lse_ref):
        qi = pl.program_id(2)
        lane = lax.broadcasted_iota(jnp.int32, (tq, LANE), 1)
        qs = [q_ref[:, hh * LANE:(hh + 1) * LANE] for hh in range(2)]

        def block(j, carry, diagonal):
            k0 = pl.multiple_of(j * tk, tk)
            vb = v_ref[pl.ds(k0, tk), :]
            new = []
            for hh in range(2):
                m, l, acc = carry[hh]
                kb = k_ref[pl.ds(k0, tk), hh * LANE:(hh + 1) * LANE]
                s = lax.dot_general(qs[hh], kb, (((1,), (1,)), ((), ())), preferred_element_type=F32)
                if diagonal:
                    s = jnp.where(_lower_triangle(tq), s, NEG)
                m_new = jnp.maximum(m, jnp.max(s, axis=-1, keepdims=True))
                p = jnp.exp(s - m_new)
                alpha = jnp.exp(m - m_new)
                l = alpha * l + jnp.sum(p, axis=-1, keepdims=True)
                acc = alpha * acc + jnp.dot(p.astype(MX), vb, preferred_element_type=F32)
                new.append((m_new, l, acc))
            return tuple(new)

        init = tuple((jnp.full((tq, 1), NEG, F32), jnp.zeros((tq, 1), F32), jnp.zeros((tq, LANE), F32)) for _ in range(2))
        carry = lax.fori_loop(0, qi, lambda j, c: block(j, c, False), init)
        (m0, l0, acc0), (m1, l1, acc1) = block(qi, carry, True)
        o_ref[...] = jnp.where(lane < NOPE, acc0 / l0, acc1 / l1).astype(o_ref.dtype)
        lse_ref[...] = jnp.where(lane < NOPE, m0 + jnp.log(l0), m1 + jnp.log(l1))

    return _pcall(body, name=name, grid=(nb, npair, nq),
                  in_specs=[pl.BlockSpec((tq, 2 * LANE), lambda b, p, i: (b * nq + i, p)),
                            pl.BlockSpec((S, 2 * LANE), lambda b, p, i: (b, p)),
                            pl.BlockSpec((S, LANE), lambda b, p, i: (b, p))],
                  out_specs=[pl.BlockSpec((tq, LANE), lambda b, p, i: (b * nq + i, p)),
                             pl.BlockSpec((tq, LANE), lambda b, p, i: (b * nq + i, p))],
                  out_shape=[_sds((T, NH * NOPE), MX), _sds((T, NH * NOPE), F32)], comm=comm)(q, k, v)


def _attn_bwd(q, k, v, o, lse, do, *, nb, name, tq=512, comm=None):
    T = q.shape[0]
    S = T // nb
    tq = _pick(S, tq)
    nq = S // tq
    tk = tq
    npair = NH // 2

    def body(q_ref, k_ref, v_ref, o_ref, lse_ref, do_ref, dq_ref, dk_ref, dv_ref, delta_ref):
        lane = lax.broadcasted_iota(jnp.int32, (tq, LANE), 1)
        first = lane < NOPE
        dq_ref[...] = jnp.zeros_like(dq_ref)

        def delta_step(qi, _):
            q0 = pl.multiple_of(qi * tq, tq)
            prod = do_ref[pl.ds(q0, tq), :] * o_ref[pl.ds(q0, tq), :].astype(F32)
            d0 = jnp.sum(jnp.where(first, prod, 0.0), axis=-1, keepdims=True)
            d1 = jnp.sum(jnp.where(first, 0.0, prod), axis=-1, keepdims=True)
            delta_ref[pl.ds(q0, tq), :] = jnp.where(first, d0, d1)
            return 0

        lax.fori_loop(0, nq, delta_step, 0)

        def kv_step(kj, _):
            k0 = pl.multiple_of(kj * tk, tk)
            kbs = [k_ref[pl.ds(k0, tk), hh * LANE:(hh + 1) * LANE] for hh in range(2)]
            vb = v_ref[pl.ds(k0, tk), :]

            def q_block(qi, carry, diagonal):
                dk0, dk1, dv = carry
                dks = [dk0, dk1]
                q0 = pl.multiple_of(qi * tq, tq)
                dov = do_ref[pl.ds(q0, tq), :]
                lse_v = lse_ref[pl.ds(q0, tq), :]
                delta_v = delta_ref[pl.ds(q0, tq), :]
                for hh in range(2):
                    qh = q_ref[pl.ds(q0, tq), hh * LANE:(hh + 1) * LANE]
                    dob = jnp.where(first if hh == 0 else jnp.logical_not(first), dov, 0.0).astype(MX)
                    s = lax.dot_general(qh, kbs[hh], (((1,), (1,)), ((), ())), preferred_element_type=F32)
                    p = jnp.exp(s - lse_v[:, hh * NOPE:hh * NOPE + 1])
                    if diagonal:
                        p = jnp.where(_lower_triangle(tq), p, 0.0)
                    dp = lax.dot_general(dob, vb, (((1,), (1,)), ((), ())), preferred_element_type=F32)
                    ds = (p * (dp - delta_v[:, hh * NOPE:hh * NOPE + 1])).astype(MX)
                    dks[hh] = dks[hh] + lax.dot_general(ds, qh, (((0,), (0,)), ((), ())), preferred_element_type=F32)
                    dv = dv + lax.dot_general(p.astype(MX), dob, (((0,), (0,)), ((), ())), preferred_element_type=F32)
                    dq_ref[pl.ds(q0, tq), hh * LANE:(hh + 1) * LANE] += jnp.dot(ds, kbs[hh], preferred_element_type=F32)
                return dks[0], dks[1], dv

            zero = jnp.zeros((tk, LANE), F32)
            carry = q_block(kj, (zero, zero, zero), True)
            dk0, dk1, dv = lax.fori_loop(kj + 1, nq, lambda qi, c: q_block(qi, c, False), carry)
            dk_ref[pl.ds(k0, tk), 0:LANE] = dk0
            dk_ref[pl.ds(k0, tk), LANE:2 * LANE] = dk1
            dv_ref[pl.ds(k0, tk), :] = dv
            return 0

        lax.fori_loop(0, nq, kv_step, 0)

    pair256 = pl.BlockSpec((S, 2 * LANE), lambda b, p: (b, p))
    pair128 = pl.BlockSpec((S, LANE), lambda b, p: (b, p))
    return _pcall(body, name=name, grid=(nb, npair),
                  in_specs=[pair256, pair256, pair128, pair128, pair128, pair128],
                  out_specs=[pair256, pair256, pair128],
                  out_shape=[_sds((T, NH * LANE), F32), _sds((T, NH * LANE), F32), _sds((T, NH * NOPE), F32)],
                  scratch=[pltpu.VMEM((S, LANE), F32)], comm=comm)(q, k, v, o, lse, do)


def _run(plan, fn, *args, name, **kw):
    rider = plan.rider(name)
    if rider is None:
        return fn(*args, name=name, **kw)
    outs, landed = fn(*args, name=name, comm=rider, **kw)
    plan.landed(name, landed)
    return outs


def _layer_fwd_bwd(x, tgt, mod3, cos, sin, plan, P):
    nb = mod3.shape[0]
    W = plan.W
    h1, gu1, a1 = _run(plan, _ffn_up_first, x, mod3, P["norm_ffn1"], W["ffn1_in"], sub=0, name="ffn1_up_a")
    gu1, a1 = _run(plan, _ffn_up_second, h1, W["ffn1_in"], gu1, a1, name="ffn1_up_b")
    f1, x1, h2 = _run(plan, _mm_resid_norm, a1, W["ffn1_out"], x, mod3, P["norm_mix"], sub=1, coef=0.5, name="ffn1_out")
    z = _run(plan, _mm, h2, W["w_in"], mode="nt", out_dtype=MX, name="mix_in", tn=1664)
    pooled, mixed, scaled = _pool_fwd(z, W["pool_grp"], P["pool_scale"], nb=nb, name="pool_fwd")
    br_pool = _mm(scaled, W["pool_proj"], mode="nn", out_dtype=MX, name="pool_proj")
    qn, kvn, qp, kvp, q, k, v = _mla_in_fwd(z, W["q_up"], W["kv_up"], P["q_a_norm"], P["kv_a_norm"], cos, sin,
                                            P["q_gain"], P["k_gain"], name="mla_in_fwd")
    attn, lse = _run(plan, _attn_fwd, q, k, v, nb=nb, name="attn_fwd")
    br_mla, merged, mo, x2, h3 = _mix_out_fwd(z, br_pool, attn, W["mla_proj"], W["w_out"], x1, mod3, P["norm_ffn2"], sub=2,
                                              name="mix_out", tm=256)
    gu2, a2 = _ffn_up(h3, W["ffn2_in"], name="ffn2_up")
    dy, df2, st_fin, loss = _mm_loss(a2, W["ffn2_out"], x2, tgt, mod3, name="ffn2_out_loss")

    plan.grad("ffn2_out", _dw(a2, df2, name="d_ffn2_out"))
    dgu2 = _ffn_dact(df2, gu2, W["ffn2_out"], name="ffn2_dact")
    plan.grad("ffn2_in", _ffn_dw_in(h3, dgu2, name="d_ffn2_in"))
    dx2, dmo, st3 = _run(plan, _ffn_dh_norm, dgu2, W["ffn2_in"], x2, dy, mo, mod3, P["norm_ffn2"], sub=2, coef=1.0,
                         name="d_ffn2_h")
    plan.grad("w_out", _dw(merged, dmo, name="d_mix_out"))
    dbr_pool, dbr_mla, dgates, dattn = _mix_out_bwd(dmo, W["w_out"], z, br_pool, br_mla, W["mla_proj"], name="mix_out_bwd", tm=256)
    plan.grad("pool_proj", _dw(scaled, dbr_pool, name="d_pool_proj"))
    dscaled = _mm(dbr_pool, W["pool_proj"], mode="nt", out_dtype=F32, name="d_pool_scaled")
    du_pool, d_pool_grp, d_pool_scale = _pool_bwd(dscaled, mixed, pooled, W["pool_grp"], P["pool_scale"], nb=nb, name="pool_bwd")
    plan.grad("mla_proj", _dw(attn, dbr_mla, name="d_mla_proj"))
    dq, dk, dv = _run(plan, _attn_bwd, q, k, v, attn, lse, dattn, nb=nb, name="attn_bwd")
    dqp, dkvp, dkr, dql, dkvl, st_prep, st_q, st_kv = _mla_in_bwd(
        dq, dk, dv, qp, kvp, z, W["q_up"], W["kv_up"], P["q_a_norm"], P["kv_a_norm"], cos, sin, P["q_gain"], P["k_gain"],
        nb=nb, name="mla_in_bwd")
    plan.grad("q_up", _dw(qn, dqp, name="d_q_up"))
    plan.grad("kv_up", _dw(kvn, dkvp, name="d_kv_up"))
    dz = jnp.concatenate([du_pool, dql, dkvl, dkr, dgates], axis=1)
    plan.grad("w_in", _run(plan, _dw, h2, dz, name="d_mix_in", tm=256, tn=1664, out_t=True))
    dx1, df1, st2 = _run(plan, _mm_norm_bwd, dz, W["w_in"], x1, dx2, f1, mod3, P["norm_mix"], sub=1, coef=0.5, name="d_mix_h")
    plan.grad("ffn1_out", _run(plan, _dw, a1, df1, name="d_ffn1_out"))
    dgu1 = _run(plan, _ffn_dact, df1, gu1, W["ffn1_out"], name="ffn1_dact")
    half = x.shape[1] // 2
    plan.grad("ffn1_in@0", _run(plan, _ffn_dw_in, h1, dgu1, rows=(0, half), name="d_ffn1_in_a"))
    plan.grad("ffn1_in@1", _run(plan, _ffn_dw_in, h1, dgu1, rows=(half, half), name="d_ffn1_in_b"))
    dh1 = _run(plan, _ffn_dh, dgu1, W["ffn1_in"], name="d_ffn1_h")
    grad_x, st1 = _run(plan, _bwd_block, x, dh1, dx1, None, mod3, P["norm_ffn1"], sub=0, coef=0.0, name="bwd_norm1")

    return loss, grad_x, (st1, st2, st3, st_fin, st_q, st_kv, st_prep, d_pool_scale), d_pool_grp


def _padded(rows):
    return rows + -rows % (4 * SUBLANE)


def _pad_rows(a):
    pad = [(0, 0)] * a.ndim
    pad[-2] = (0, _padded(a.shape[-2]) - a.shape[-2])
    return jnp.pad(a, pad)


def _w_in_pieces(shard):
    runs = []
    for ref0, ref1, k0 in ((0, 1152, 0), (1152, 1184, Z_KR + KR_LANE), (1184, N_CHIP * shard, Z_GP)):
        for q in range(N_CHIP):
            lo, hi = max(ref0, q * shard), min(ref1, (q + 1) * shard)
            if lo < hi:
                runs.append((k0 + lo - ref0, q * _padded(shard) + lo - q * shard, hi - lo))
    return runs


def _w_in_stacked_to_kernel(st, shard):
    flat = st.reshape(-1, st.shape[2])
    parts, at = [], 0
    for k0, s0, n in sorted(_w_in_pieces(shard)):
        parts += [jnp.zeros((k0 - at, flat.shape[1]), st.dtype)] * (k0 > at) + [flat[s0:s0 + n]]
        at = k0 + n
    assert at == Z_W
    return jnp.concatenate(parts, axis=0)


def _w_in_kernel_to_stacked(gt, shard):
    parts, at = [], 0
    for s0, k0, n in sorted((s, k, n) for k, s, n in _w_in_pieces(shard)) + [(N_CHIP * _padded(shard), 0, 0)]:
        parts += [jnp.zeros((s0 - at, gt.shape[1]), gt.dtype)] * (s0 > at) + [gt[k0:k0 + n]] * (n > 0)
        at = s0 + n
    return jnp.concatenate(parts, axis=0).reshape(N_CHIP, _padded(shard), gt.shape[1])


def _q_up_to_kernel(w):
    k = w.shape[0]
    return jnp.pad(w.reshape(k, NH, NOPE + ROPE), ((0, 0), (0, 0), (0, LANE - NOPE - ROPE))).reshape(k, NH * LANE)


def _q_up_from_kernel(g):
    k = g.shape[0]
    return g.reshape(k, NH, LANE)[:, :, :NOPE + ROPE].reshape(k, NH * (NOPE + ROPE))


def _kv_up_to_kernel(w):
    k = w.shape[0]
    w3 = w.reshape(k, NH, 2 * NOPE)
    kpart = jnp.pad(w3[:, :, :NOPE], ((0, 0), (0, 0), (0, LANE - NOPE))).reshape(k, NH * LANE)
    return jnp.concatenate([kpart, w3[:, :, NOPE:].reshape(k, NH * NOPE)], axis=1)


def _kv_up_from_kernel(g):
    k = g.shape[0]
    kpart = g[:, :NH * LANE].reshape(k, NH, LANE)[:, :, :NOPE]
    vpart = g[:, NH * LANE:].reshape(k, NH, NOPE)
    return jnp.concatenate([kpart, vpart], axis=2).reshape(k, NH * 2 * NOPE)


def _gain_slab(nope, rope):
    return jnp.concatenate([nope, rope, jnp.zeros((1, LANE - NOPE - ROPE), nope.dtype)], axis=1)


def _rope_tables(positions):
    inv_freq = 10000.0 ** (-jnp.arange(0, ROPE, 2, dtype=F32) / ROPE)
    ang = positions.astype(F32)[:, None] * inv_freq
    ang = jnp.concatenate([ang, ang], axis=-1)
    t = positions.shape[0]
    cos = jnp.concatenate([jnp.ones((t, KR_LANE), F32), jnp.cos(ang), jnp.ones((t, LANE - KR_LANE - ROPE), F32)], axis=1)
    sin = jnp.concatenate([jnp.zeros((t, KR_LANE), F32), jnp.sin(ang), jnp.zeros((t, LANE - KR_LANE - ROPE), F32)], axis=1)
    return cos, sin


def _coords():
    return lax.axis_index("x"), lax.axis_index("y"), lax.axis_index("c")


HBM_SPEC = pl.BlockSpec(memory_space=pl.ANY)
VMEM_SPEC = pl.BlockSpec(memory_space=pltpu.VMEM)


CHIP_RELS = ((1, 0), (0, 1), (1, 1))


def _gather_comm(shards):
    n = len(shards)

    def ici(ins, outs, sems, a, j, x, y, cc):
        half = ins[a].shape[0] // 2
        mine = pl.ds(cc * half, half)
        dx, dy = CHIP_RELS[j]
        return pltpu.make_async_remote_copy(src_ref=ins[a].at[mine], dst_ref=outs[a].at[2 * x + y, mine],
                                            send_sem=sems[0].at[a, j], recv_sem=sems[1].at[a, j],
                                            device_id=(x ^ dx, y ^ dy, cc), device_id_type=MESH)

    def d2d(ins, outs, sems, a, j, x, y, cc, half_of):
        half = ins[a].shape[0] // 2
        dx, dy = CHIP_RELS[j]
        landed = outs[a].at[2 * (x ^ dx) + (y ^ dy), pl.ds(half_of * half, half)]
        return pltpu.make_async_remote_copy(src_ref=landed, dst_ref=landed, send_sem=sems[2].at[a, j], recv_sem=sems[3].at[a, j],
                                            device_id=(x, y, 1 - cc), device_id_type=MESH)

    def own(ins, outs, sems, a, x, y):
        return pltpu.make_async_copy(ins[a], outs[a].at[2 * x + y], sems[4].at[a])

    def start(ins, outs, sems):
        x, y, cc = _coords()
        for a in range(n):
            own(ins, outs, sems, a, x, y).start()
            for j in range(3):
                ici(ins, outs, sems, a, j, x, y, cc).start()

    def finish(ins, outs, sems):
        x, y, cc = _coords()
        for a in range(n):
            for j in range(3):
                ici(ins, outs, sems, a, j, x, y, cc).wait_recv()
                d2d(ins, outs, sems, a, j, x, y, cc, cc).start()
        for a in range(n):
            for j in range(3):
                d2d(ins, outs, sems, a, j, x, y, cc, 1 - cc).wait_recv()
        for a in range(n):
            for j in range(3):
                ici(ins, outs, sems, a, j, x, y, cc).wait_send()
                d2d(ins, outs, sems, a, j, x, y, cc, cc).wait_send()
            own(ins, outs, sems, a, x, y).wait()

    dma = pltpu.SemaphoreType.DMA
    return _Comm(shards, [_sds((N_CHIP,) + s.shape, s.dtype) for s in shards],
                 [dma((n, 3)), dma((n, 3)), dma((n, 3)), dma((n, 3)), dma((n,))], start, finish)


def _swap_comm(parts):
    n = len(parts)

    def copy(ins, outs, sems, a):
        x, y, cc = _coords()
        half = ins[a].shape[1] // 2
        return pltpu.make_async_remote_copy(src_ref=ins[a].at[:, pl.ds((1 - cc) * half, half)], dst_ref=outs[a],
                                            send_sem=sems[0].at[a], recv_sem=sems[1].at[a], device_id=(x, y, 1 - cc),
                                            device_id_type=MESH)

    def start(ins, outs, sems):
        for a in range(n):
            copy(ins, outs, sems, a).start()

    def finish(ins, outs, sems):
        for a in range(n):
            copy(ins, outs, sems, a).wait()

    dma = pltpu.SemaphoreType.DMA
    return _Comm(parts, [_sds((p.shape[0], p.shape[1] // 2, p.shape[2]), p.dtype) for p in parts], [dma((n,)), dma((n,))],
                 start, finish)


def _add_half(full, other, cidx, *, name):
    nch, r, c = full.shape
    half = r // 2
    tr = _pick_rows(half)
    nbk = half // tr
    grid_spec = pltpu.PrefetchScalarGridSpec(
        num_scalar_prefetch=1, grid=(nch, nbk),
        in_specs=[pl.BlockSpec((1, tr, c), lambda j, i, cref: (j, cref[0] * nbk + i, 0)),
                  pl.BlockSpec((1, tr, c), lambda j, i, cref: (j, i, 0))],
        out_specs=pl.BlockSpec((1, tr, c), lambda j, i, cref: (j, i, 0)))

    def body(cref, a_ref, b_ref, o_ref):
        o_ref[...] = (a_ref[...] + b_ref[...]).astype(o_ref.dtype)

    return _pcall(body, name=name, out_shape=_sds((nch, half, c), MX), grid_spec=grid_spec)(cidx, full, other)


def _pick_rows(rows, target=512):
    best = None
    for t in range(16, min(rows, target) + 1, 16):
        if rows % t == 0:
            best = t
    return rows if best is None else best


def _exchange_comm(parts):
    n = len(parts)

    def send(ins, outs, sems, a, j, x, y, cc):
        dx, dy = CHIP_RELS[j]
        return pltpu.make_async_remote_copy(src_ref=ins[a].at[2 * (x ^ dx) + (y ^ dy)], dst_ref=outs[a].at[2 * x + y],
                                            send_sem=sems[0].at[a, j], recv_sem=sems[1].at[a, j],
                                            device_id=(x ^ dx, y ^ dy, cc), device_id_type=MESH)

    def landing(ins, outs, sems, a, j, x, y, cc):
        dx, dy = CHIP_RELS[j]
        peer_chip = 2 * (x ^ dx) + (y ^ dy)
        return pltpu.make_async_remote_copy(src_ref=ins[a].at[peer_chip], dst_ref=outs[a].at[peer_chip], send_sem=sems[0].at[a, j],
                                            recv_sem=sems[1].at[a, j], device_id=(x, y, cc), device_id_type=MESH)

    def own(ins, outs, sems, a, x, y):
        return pltpu.make_async_copy(ins[a].at[2 * x + y], outs[a].at[2 * x + y], sems[2].at[a])

    def start(ins, outs, sems):
        x, y, cc = _coords()
        for a in range(n):
            own(ins, outs, sems, a, x, y).start()
            for j in range(3):
                send(ins, outs, sems, a, j, x, y, cc).start()

    def finish(ins, outs, sems):
        x, y, cc = _coords()
        for a in range(n):
            for j in range(3):
                landing(ins, outs, sems, a, j, x, y, cc).wait_recv()
        for a in range(n):
            for j in range(3):
                send(ins, outs, sems, a, j, x, y, cc).wait_send()
            own(ins, outs, sems, a, x, y).wait()

    dma = pltpu.SemaphoreType.DMA
    return _Comm(parts, [_sds(p.shape, p.dtype) for p in parts], [dma((n, 3)), dma((n, 3)), dma((n,))], start, finish)


def _sum_chips(q, cidx, *, name):
    nch, h, c = q.shape
    tr = _pick_rows(h)
    nbk = h // tr
    grid_spec = pltpu.PrefetchScalarGridSpec(
        num_scalar_prefetch=1, grid=(nbk,),
        in_specs=[pl.BlockSpec((nch, tr, c), lambda i, cref: (0, i, 0))],
        out_specs=pl.BlockSpec((tr, c), lambda i, cref: (cref[0] * nbk + i, 0)))

    def body(cref, q_ref, o_ref):
        acc = q_ref[0].astype(F32) + q_ref[1].astype(F32)
        acc = acc + q_ref[2].astype(F32)
        o_ref[...] = acc + q_ref[3].astype(F32)

    return _pcall(body, name=name, out_shape=_sds((2 * h, c), F32), grid_spec=grid_spec)(cidx, q)


def _join_comm(fulls):
    n = len(fulls)

    def half_copy(outs, sems, a, which):
        x, y, cc = _coords()
        h = outs[a].shape[0] // 2
        rows = outs[a].at[pl.ds((cc if which == 0 else 1 - cc) * h, h)]
        return pltpu.make_async_remote_copy(src_ref=rows, dst_ref=rows, send_sem=sems[0].at[a], recv_sem=sems[1].at[a],
                                            device_id=(x, y, 1 - cc), device_id_type=MESH)

    def start(ins, outs, sems):
        for a in range(n):
            half_copy(outs, sems, a, 0).start()

    def finish(ins, outs, sems):
        for a in range(n):
            half_copy(outs, sems, a, 1).wait_recv()
        for a in range(n):
            half_copy(outs, sems, a, 0).wait_send()

    dma = pltpu.SemaphoreType.DMA
    return _Comm(fulls, [_sds(p.shape, p.dtype) for p in fulls], [dma((n,)), dma((n,))], start, finish,
                 aliases={a: a for a in range(n)})


def _ada_prologue(c, w, b, *, name, comm=None):
    nb, dm = c.shape
    n = w.shape[1]

    def body(c_ref, w_ref, b_ref, call_ref, land_ref, cpad, mod_s, g_send, g_recv, m_send, m_recv):
        x, y, cc = _coords()
        me = 4 * x + 2 * y + cc
        chip = 2 * x + y
        cpad[...] = jnp.zeros_like(cpad)
        cpad[0:nb, :] = c_ref[...]
        copies = []
        for kk in range(1, N_DEV):
            peer = (x ^ (kk >> 2), y ^ ((kk >> 1) & 1), cc ^ (kk & 1))
            cp = pltpu.make_async_remote_copy(src_ref=cpad, dst_ref=call_ref.at[me], send_sem=g_send.at[kk - 1],
                                              recv_sem=g_recv.at[kk - 1], device_id=peer, device_id_type=MESH)
            cp.start()
            copies.append(cp)
        call_ref[me] = cpad[...]
        for kk in range(1, N_DEV):
            pltpu.make_async_remote_copy(src_ref=cpad, dst_ref=call_ref.at[me ^ kk], send_sem=g_send.at[kk - 1],
                                         recv_sem=g_recv.at[kk - 1], device_id=(x, y, cc), device_id_type=MESH).wait_recv()
        cv = call_ref[...].reshape(N_DEV * SUBLANE, dm)
        act = (cv * _sigmoid(cv)).astype(MX)
        mod = jnp.dot(act, w_ref[...].astype(MX), preferred_element_type=F32) + b_ref[...]
        mod_s[...] = mod.reshape(N_DEV, SUBLANE, n)
        for j, (dx, dy) in enumerate(CHIP_RELS):
            cp = pltpu.make_async_remote_copy(src_ref=mod_s.at[4 * (x ^ dx) + 2 * (y ^ dy) + cc], dst_ref=land_ref.at[chip],
                                              send_sem=m_send.at[j], recv_sem=m_recv.at[j],
                                              device_id=(x ^ dx, y ^ dy, cc), device_id_type=MESH)
            cp.start()
            copies.append(cp)
        land_ref[chip] = mod_s[me]
        for j, (dx, dy) in enumerate(CHIP_RELS):
            pltpu.make_async_remote_copy(src_ref=mod_s.at[me], dst_ref=land_ref.at[2 * (x ^ dx) + (y ^ dy)], send_sem=m_send.at[j],
                                         recv_sem=m_recv.at[j], device_id=(x, y, cc), device_id_type=MESH).wait_recv()
        for cp in copies:
            cp.wait_send()

    dma = pltpu.SemaphoreType.DMA
    return _pcall(body, name=name, in_specs=[VMEM_SPEC] * 3, out_specs=[VMEM_SPEC] * 2,
                  out_shape=[_sds((N_DEV, SUBLANE, dm), F32), _sds((N_CHIP, SUBLANE, n), F32)],
                  scratch=[pltpu.VMEM((SUBLANE, dm), F32), pltpu.VMEM((N_DEV, SUBLANE, n), F32),
                           dma((N_DEV - 1,)), dma((N_DEV - 1,)), dma((3,)), dma((3,))], comm=comm)(c, w, b)


def _ada_bwd(c_all, dmod_cols, *, name):
    m, kdim = c_all.shape
    n = dmod_cols.shape[1]
    tn = _pick(n, 1152)

    def body(c_ref, d_ref, o_ref):
        cv = c_ref[...]
        act = (cv * _sigmoid(cv)).astype(MX)
        o_ref[...] = lax.dot_general(act, d_ref[...].astype(MX), (((0,), (0,)), ((), ())), preferred_element_type=F32)

    return _pcall(body, name=name, out_shape=_sds((kdim, n), F32), grid=(n // tn,),
                  in_specs=[pl.BlockSpec((m, kdim), lambda j: (0, 0)), pl.BlockSpec((m, tn), lambda j: (0, j))],
                  out_specs=pl.BlockSpec((kdim, tn), lambda j: (0, j)))(c_all, dmod_cols)


SLAB_W = 1024
RED_ROWS = 8


LOSS_LANE = SLAB_W - LANE


def _pack_stats(st1, st2, st3, st_fin, st_q, st_kv, st_prep, d_pool_scale, loss8, *, name):
    nb = st1.shape[0]
    dm_rows = -(-9 * nb // SUBLANE) * SUBLANE

    def body(s1, s2, s3, sf, sq, skv, sp, sps, loss_ref, o_ref):
        o_ref[...] = jnp.zeros_like(o_ref)
        for s in range(nb):
            rows = [s1[s, 0:1, :], s1[s, 1:2, :], s2[s, 3:4, :], s2[s, 0:1, :], s2[s, 1:2, :], s3[s, 3:4, :], s3[s, 0:1, :],
                    s3[s, 1:2, :], sf[s, 0:1, :]]
            for k, row in enumerate(rows):
                o_ref[9 * s + k:9 * s + k + 1, :] = row

        def over_seq(ref, r):
            acc = ref[0, r:r + 1, :]
            for s in range(1, nb):
                acc = acc + ref[s, r:r + 1, :]
            return acc

        o_ref[dm_rows + 0:dm_rows + 1, :] = over_seq(s1, 2)
        o_ref[dm_rows + 1:dm_rows + 2, :] = over_seq(s2, 2)
        o_ref[dm_rows + 2:dm_rows + 3, :] = over_seq(s3, 2)
        o_ref[dm_rows + 3:dm_rows + 4, 0:POOL_W] = over_seq(sps, 0)
        o_ref[dm_rows + 4:dm_rows + 5, 0:QL] = over_seq(sq, 0)
        o_ref[dm_rows + 5:dm_rows + 6, 0:KVL] = over_seq(skv, 0)
        o_ref[dm_rows + 6:dm_rows + 7, 0:LANE] = over_seq(sp, 0)
        o_ref[dm_rows + 7:dm_rows + 8, 0:LANE] = over_seq(sp, 1)
        o_ref[dm_rows + 7:dm_rows + 8, LOSS_LANE:] = loss_ref[0:1, :]

    return _pcall(body, name=name, out_shape=_sds((dm_rows + RED_ROWS, SLAB_W), F32), in_specs=[VMEM_SPEC] * 9,
                  out_specs=VMEM_SPEC)(st1, st2, st3, st_fin, st_q, st_kv, st_prep, d_pool_scale, loss8)


def _small_allreduce(slab, pool, *, name, comm=None):
    rows, w = slab.shape
    dm = rows - RED_ROWS
    prow, pw = pool.shape
    crow = RED_ROWS + 2 * dm

    def body(slab_ref, pool_ref, red_ref, dmod_ref, ptot_ref, sib_slab, sib_pool, chip_slab, chip_pool, land_slab, land_pool,
             a_send, a_recv, b_send, b_recv):
        x, y, cc = _coords()
        chip = 2 * x + y
        sib = (x, y, 1 - cc)
        to_sib = [pltpu.make_async_remote_copy(src_ref=slab_ref, dst_ref=sib_slab, send_sem=a_send.at[0], recv_sem=a_recv.at[0],
                                               device_id=sib, device_id_type=MESH),
                  pltpu.make_async_remote_copy(src_ref=pool_ref, dst_ref=sib_pool, send_sem=a_send.at[1], recv_sem=a_recv.at[1],
                                               device_id=sib, device_id_type=MESH)]
        for cp in to_sib:
            cp.start()
        for cp in to_sib:
            cp.wait()
        mine_dm, theirs_dm = slab_ref[0:dm, :], sib_slab[0:dm, :]
        chip_slab[0:RED_ROWS, :] = slab_ref[dm:, :] + sib_slab[dm:, :]
        chip_slab[RED_ROWS:RED_ROWS + dm, :] = jnp.where(cc == 0, mine_dm, theirs_dm)
        chip_slab[RED_ROWS + dm:, :] = jnp.where(cc == 0, theirs_dm, mine_dm)
        chip_pool[...] = pool_ref[...] + sib_pool[...]

        sends = []
        for j, (dx, dy) in enumerate(CHIP_RELS):
            peer = (x ^ dx, y ^ dy, cc)
            sends.append(pltpu.make_async_remote_copy(src_ref=chip_slab, dst_ref=land_slab.at[chip], send_sem=b_send.at[j, 0],
                                                      recv_sem=b_recv.at[j, 0], device_id=peer, device_id_type=MESH))
            sends.append(pltpu.make_async_remote_copy(src_ref=chip_pool, dst_ref=land_pool.at[chip], send_sem=b_send.at[j, 1],
                                                      recv_sem=b_recv.at[j, 1], device_id=peer, device_id_type=MESH))
        for cp in sends:
            cp.start()
        land_slab[chip] = chip_slab[...]
        land_pool[chip] = chip_pool[...]
        for j, (dx, dy) in enumerate(CHIP_RELS):
            peer_chip = 2 * (x ^ dx) + (y ^ dy)
            pltpu.make_async_remote_copy(src_ref=chip_slab, dst_ref=land_slab.at[peer_chip], send_sem=b_send.at[j, 0],
                                         recv_sem=b_recv.at[j, 0], device_id=(x, y, cc), device_id_type=MESH).wait_recv()
            pltpu.make_async_remote_copy(src_ref=chip_pool, dst_ref=land_pool.at[peer_chip], send_sem=b_send.at[j, 1],
                                         recv_sem=b_recv.at[j, 1], device_id=(x, y, cc), device_id_type=MESH).wait_recv()
        red = land_slab[0, 0:RED_ROWS, :]
        ptot = land_pool[0]
        for ch in range(1, N_CHIP):
            red = red + land_slab[ch, 0:RED_ROWS, :]
            ptot = ptot + land_pool[ch]
        red_ref[...] = red
        ptot_ref[...] = ptot
        for ch in range(N_CHIP):
            dmod_ref[2 * dm * ch:2 * dm * (ch + 1), :] = land_slab[ch, RED_ROWS:, :]
        for cp in sends:
            cp.wait_send()

    dma = pltpu.SemaphoreType.DMA
    return _pcall(body, name=name, in_specs=[VMEM_SPEC, VMEM_SPEC], out_specs=[VMEM_SPEC] * 3,
                  out_shape=[_sds((RED_ROWS, w), F32), _sds((N_DEV * dm, w), F32), _sds((prow, pw), F32)],
                  scratch=[pltpu.VMEM((rows, w), F32), pltpu.VMEM((prow, pw), F32), pltpu.VMEM((crow, w), F32),
                           pltpu.VMEM((prow, pw), F32), pltpu.VMEM((N_CHIP, crow, w), F32), pltpu.VMEM((N_CHIP, prow, pw), F32),
                           dma((2,)), dma((2,)), dma((3, 2)), dma((3, 2))], comm=comm)(slab, pool)


def _adamw_math(w, g, m, v):
    mn = ADAM_B1 * m + (1.0 - ADAM_B1) * g
    vn = ADAM_B2 * v + (1.0 - ADAM_B2) * (g * g)
    bc1 = 1.0 / (1.0 - ADAM_B1 ** ADAM_STEP)
    bc2 = 1.0 / (1.0 - ADAM_B2 ** ADAM_STEP)
    return -ADAM_LR * ((mn * bc1) / (jnp.sqrt(vn * bc2) + ADAM_EPS) + ADAM_WD * w), mn, vn


def _small_update(red, dmod_all, pool_total, nb, params, *, name):
    names = list(SMALL)
    dm = dmod_all.shape[0] // N_DEV

    def grad_of(nm, red_ref, dmod_ref, ptot_ref):
        if nm == "b_ada":
            acc = None
            for d in range(N_DEV):
                for s in range(nb):
                    blk = dmod_ref[d * dm + 9 * s:d * dm + 9 * s + 9, :]
                    acc = blk if acc is None else acc + blk
            return jnp.concatenate([acc[k:k + 1, :] for k in range(9)], axis=1)
        if nm == "pool_grp":
            return ptot_ref[...]
        row, lo, n = {"norm_ffn1": (0, 0, D), "norm_mix": (1, 0, D), "norm_ffn2": (2, 0, D), "pool_scale": (3, 0, POOL_W),
                      "q_a_norm": (4, 0, QL), "kv_a_norm": (5, 0, KVL), "q_norm_nope": (6, 0, NOPE),
                      "q_norm_rope": (6, NOPE, ROPE), "k_norm_nope": (7, 0, NOPE), "k_norm_rope": (7, KR_LANE, ROPE)}[nm]
        return red_ref[row:row + 1, lo:lo + n]

    def body(*refs):
        red_ref, dmod_ref, ptot_ref = refs[:3]
        ins = refs[3:3 + 3 * len(names)]
        outs = refs[3 + 3 * len(names):]
        outs[4 * len(names)][...] = red_ref[RED_ROWS - 1:RED_ROWS, LOSS_LANE:]
        for i, nm in enumerate(names):
            g = grad_of(nm, red_ref, dmod_ref, ptot_ref)
            d, mn, vn = _adamw_math(ins[3 * i][...], g, ins[3 * i + 1][...], ins[3 * i + 2][...])
            outs[4 * i][...] = g
            outs[4 * i + 1][...] = d
            outs[4 * i + 2][...] = mn
            outs[4 * i + 3][...] = vn

    flat_in = [a for nm in names for a in params[nm]]
    out_shape = [_sds(params[nm][0].shape, F32) for nm in names for _ in range(4)] + [_sds((1, LANE), F32)]
    res = _pcall(body, name=name, in_specs=[VMEM_SPEC] * (3 + len(flat_in)), out_specs=[VMEM_SPEC] * len(out_shape),
                 out_shape=out_shape)(red, dmod_all, pool_total, *flat_in)
    return {nm: tuple(res[4 * i:4 * i + 4]) for i, nm in enumerate(names)}, res[-1]


def _adamw(w, g, m, v, *, name, comm=None):
    r, c = w.shape
    tr = _pick_rows(r, 256)
    tc = c if tr < r else _pick(c, 256)
    spec = pl.BlockSpec((tr, tc), lambda i, j: (i, j))
    bc1 = 1.0 / (1.0 - ADAM_B1 ** ADAM_STEP)
    bc2 = 1.0 / (1.0 - ADAM_B2 ** ADAM_STEP)

    def body(w_ref, g_ref, m_ref, v_ref, d_ref, mo_ref, vo_ref):
        gv = g_ref[...]
        mn = ADAM_B1 * m_ref[...] + (1.0 - ADAM_B1) * gv
        vn = ADAM_B2 * v_ref[...] + (1.0 - ADAM_B2) * (gv * gv)
        mo_ref[...] = mn
        vo_ref[...] = vn
        d_ref[...] = -ADAM_LR * ((mn * bc1) / (jnp.sqrt(vn * bc2) + ADAM_EPS) + ADAM_WD * w_ref[...])

    out = _sds((r, c), F32)
    return _pcall(body, name=name, out_shape=[out, out, out], grid=(r // tr, c // tc), in_specs=[spec] * 4, out_specs=[spec] * 3,
                  comm=comm)(w, g, m, v)


BIG = ("w_ffn1_in", "w_ffn1_out", "w_in", "w_pool_proj", "w_q_up", "w_kv_up", "w_mla_proj", "w_out", "w_ffn2_in", "w_ffn2_out")
ROW_SHARDED = ("w_ffn1_out", "w_out", "w_ffn2_out")
KERNEL_NAME = {"w_ffn1_in": "ffn1_in", "w_ffn1_out": "ffn1_out", "w_in": "w_in", "w_pool_proj": "pool_proj", "w_q_up": "q_up",
               "w_kv_up": "kv_up", "w_mla_proj": "mla_proj", "w_out": "w_out", "w_ffn2_in": "ffn2_in", "w_ffn2_out": "ffn2_out"}
WEIGHTS = ("w_ada", "b_ada", "norm_ffn1", "w_ffn1_in", "w_ffn1_out", "norm_mix", "w_in", "pool_grp", "pool_scale", "w_pool_proj",
           "q_a_norm", "w_q_up", "kv_a_norm", "w_kv_up", "q_norm_nope", "q_norm_rope", "k_norm_nope", "k_norm_rope", "w_mla_proj",
           "w_out", "norm_ffn2", "w_ffn2_in", "w_ffn2_out")
SMALL = ("b_ada", "norm_ffn1", "norm_mix", "pool_grp", "pool_scale", "q_a_norm", "kv_a_norm", "q_norm_nope", "q_norm_rope",
         "k_norm_nope", "k_norm_rope", "norm_ffn2")


def _assemble(name, stacked):
    if name in ROW_SHARDED:
        return stacked.reshape(stacked.shape[0] * stacked.shape[1], stacked.shape[2])
    return jnp.transpose(stacked, (1, 0, 2)).reshape(stacked.shape[1], stacked.shape[0] * stacked.shape[2])


def _split(name, full):
    if name in ROW_SHARDED:
        return full.reshape(N_CHIP, full.shape[0] // N_CHIP, full.shape[1])
    return jnp.transpose(full.reshape(full.shape[0], N_CHIP, full.shape[1] // N_CHIP), (1, 0, 2))


GU = ("w_ffn1_in", "w_ffn2_in")
NARROW = ("w_in", "w_q_up")
TRANSPOSED = ("w_in",)
W_IN_SHARD = (Z_W - LANE + ROPE) // N_CHIP


MIX_SMALL = ("w_out", "w_pool_proj", "w_mla_proj", "w_q_up", "w_kv_up")
RIDES = {
    "ada_prologue": (("gather", ("w_ffn1_in",)),),
    "ffn1_up_a": (("gather", ("w_ffn1_out",)),),
    "ffn1_up_b": (("gather", ("w_in",)),),
    "mix_in": (("gather", MIX_SMALL),),
    "attn_fwd": (("gather", ("w_ffn2_in", "w_ffn2_out")),),
    "d_ffn2_h": (("swap", ("w_ffn2_out", "w_ffn2_in")),),
    "attn_bwd": (("exchange", ("w_ffn2_out", "w_ffn2_in")),),
    "d_mix_in": (("swap", MIX_SMALL),),
    "d_mix_h": (("exchange", MIX_SMALL), ("swap", ("w_in",))),
    "ffn1_dact": (("exchange", ("w_in",)), ("swap", ("w_ffn1_out",))),
    "d_ffn1_in_a": (("exchange", ("w_ffn1_out",)),),
    "d_ffn1_in_b": (("swap", ("w_ffn1_in@0",)),),
    "d_ffn1_h": (("exchange", ("w_ffn1_in@0",)), ("swap", ("w_ffn1_in@1",))),
    "bwd_norm1": (("exchange", ("w_ffn1_in@1",)),),
    "small_allreduce": (("join", tuple(n for n in BIG if n != "w_ffn1_in") + ("w_ffn1_in@0", "w_ffn1_in@1")),),
}


def _kname(n):
    base, _, part = n.partition("@")
    return KERNEL_NAME[base] + ("@" + part if part else "")


def _both(comms):
    if len(comms) == 1:
        return comms[0]
    ins, outs, sems, aliases, spans = [], [], [], {}, []
    for c in comms:
        spans.append((len(ins), len(c.ins), len(outs), len(c.out_shapes), len(sems), len(c.sems)))
        aliases.update({len(ins) + i: len(outs) + o for i, o in c.aliases.items()})
        ins, outs, sems = ins + c.ins, outs + c.out_shapes, sems + c.sems

    def each(which):
        def run(i_, o_, s_):
            for c, (ia, ni, oa, no, sa, ns) in zip(comms, spans):
                getattr(c, which)(i_[ia:ia + ni], o_[oa:oa + no], s_[sa:sa + ns])
        return run

    return _Comm(ins, outs, sems, each("start"), each("finish"), aliases)


class _ExchangePlan:
    def __init__(self, shards, cidx):
        self.shards, self.cidx = shards, cidx
        self.W, self.G, self.parts, self.pre, self.reduced, self.joined = {}, {}, {}, {}, {}, {}

    def grad(self, key, g):
        self.G[key] = g

    def rider(self, name):
        comms = []
        for kind, names in RIDES.get(name, ()):
            if kind == "gather":
                comms.append(_gather_comm([self.shards[n] for n in names]))
            elif kind == "swap":
                for n in names:
                    self.parts[n] = self._stacked(n)
                comms.append(_swap_comm([self.parts[n] for n in names]))
            elif kind == "exchange":
                comms.append(_exchange_comm([self.pre[n] for n in names]))
            else:
                comms.append(_join_comm([self.reduced[n] for n in names]))
        return _both(comms) if comms else None

    def landed(self, name, outs):
        at = 0
        for kind, names in RIDES[name]:
            for n, o in zip(names, outs[at:at + len(names)]):
                if kind == "gather":
                    self.W[KERNEL_NAME[n]] = self._to_kernel(n, o)
                elif kind == "swap":
                    self.pre[n] = _add_half(self.parts[n], o, self.cidx, name="rs_add_" + _kname(n))
                elif kind == "exchange":
                    self.reduced[n] = _sum_chips(o, self.cidx, name="rs_sum_" + _kname(n))
                else:
                    self.joined[n] = o
            at += len(names)

    @staticmethod
    def _to_kernel(n, stacked):
        if n in GU:
            return stacked
        if n in TRANSPOSED:
            return _w_in_stacked_to_kernel(stacked, W_IN_SHARD)
        full = _assemble(n, stacked)
        return {"w_q_up": _q_up_to_kernel, "w_kv_up": _kv_up_to_kernel}.get(n, lambda w: w)(full)

    def _stacked(self, n):
        g = self.G[_kname(n)]
        if n.partition("@")[0] in GU:
            return g
        if n in TRANSPOSED:
            return _w_in_kernel_to_stacked(g, W_IN_SHARD)
        full = {"w_q_up": _q_up_from_kernel, "w_kv_up": _kv_up_from_kernel}.get(n, lambda w: w)(g)
        return _split(n, full)


def kernel(x, c, positions, w_ada, b_ada, norm_ffn1, w_ffn1_in, w_ffn1_out, norm_mix, w_in, pool_grp, pool_scale, w_pool_proj, q_a_norm, w_q_up, kv_a_norm, w_kv_up, q_norm_nope, q_norm_rope, k_norm_nope, k_norm_rope, w_mla_proj, w_out, norm_ffn2, w_ffn2_in, w_ffn2_out, loss_target, m_w_ada, m_b_ada, m_norm_ffn1, m_w_ffn1_in, m_w_ffn1_out, m_norm_mix, m_w_in, m_pool_grp, m_pool_scale, m_w_pool_proj, m_q_a_norm, m_w_q_up, m_kv_a_norm, m_w_kv_up, m_q_norm_nope, m_q_norm_rope, m_k_norm_nope, m_k_norm_rope, m_w_mla_proj, m_w_out, m_norm_ffn2, m_w_ffn2_in, m_w_ffn2_out, v_w_ada, v_b_ada, v_norm_ffn1, v_w_ffn1_in, v_w_ffn1_out, v_norm_mix, v_w_in, v_pool_grp, v_pool_scale, v_w_pool_proj, v_q_a_norm, v_w_q_up, v_kv_a_norm, v_w_kv_up, v_q_norm_nope, v_q_norm_rope, v_k_norm_nope, v_k_norm_rope, v_w_mla_proj, v_w_out, v_norm_ffn2, v_w_ffn2_in, v_w_ffn2_out):
    args = dict(locals())
    wts = {n: args[n][0] for n in WEIGHTS}
    mom = {n: args["m_" + n][0] for n in WEIGHTS}
    var = {n: args["v_" + n][0] for n in WEIGHTS}
    nb, seq, dm = x.shape
    tokens = nb * seq
    xi, yi, ci = _coords()
    chip = 2 * xi + yi

    cidx = ci.astype(jnp.int32).reshape(1)
    plan = _ExchangePlan({n: _pad_rows(wts[n].T.astype(MX)) if n in TRANSPOSED else wts[n].astype(MX) for n in BIG}, cidx)
    plan.W["pool_grp"] = wts["pool_grp"].astype(MX)

    ncol = w_ada.shape[2]
    b_cols = lax.dynamic_slice_in_dim(wts["b_ada"].reshape(1, -1), chip * ncol, ncol, axis=1)
    c_slots, mod_slots = _run(plan, _ada_prologue, c, wts["w_ada"], b_cols, name="ada_prologue")
    c_all = c_slots[:, :nb].reshape(N_DEV * nb, dm)
    mod3 = jnp.transpose(mod_slots[:, :nb], (1, 0, 2)).reshape(nb, 9, dm)
    P = {"norm_ffn1": wts["norm_ffn1"].reshape(1, dm), "norm_mix": wts["norm_mix"].reshape(1, dm),
         "norm_ffn2": wts["norm_ffn2"].reshape(1, dm), "pool_scale": wts["pool_scale"].reshape(1, POOL_W),
         "q_a_norm": wts["q_a_norm"].reshape(1, QL), "kv_a_norm": wts["kv_a_norm"].reshape(1, KVL),
         "q_gain": _gain_slab(wts["q_norm_nope"].reshape(1, NOPE), wts["q_norm_rope"].reshape(1, ROPE)),
         "k_gain": _gain_slab(wts["k_norm_nope"].reshape(1, NOPE), wts["k_norm_rope"].reshape(1, ROPE))}
    cos, sin = _rope_tables(positions.reshape(tokens))

    loss8, grad_x, stats, d_pool_grp = _layer_fwd_bwd(x.reshape(tokens, dm), loss_target.reshape(tokens, dm), mod3, cos, sin, plan, P)

    slab = _pack_stats(*stats, loss8, name="pack_stats")
    red, dmod_rows, pool_total = _run(plan, _small_allreduce, slab, d_pool_grp.reshape(POOL_G * LANE, LANE), name="small_allreduce")
    grads_t = {n: plan.joined[n][:W_IN_SHARD] for n in TRANSPOSED}
    grads = {n: grads_t[n].T if n in TRANSPOSED else plan.joined[n] for n in BIG if n != "w_ffn1_in"}
    grads["w_ffn1_in"] = jnp.concatenate([plan.joined["w_ffn1_in@0"], plan.joined["w_ffn1_in@1"]], axis=0)
    dm_rows = dmod_rows.shape[0] // N_DEV
    dmod_all = dmod_rows.reshape(N_DEV, dm_rows, SLAB_W)[:, :9 * nb].reshape(N_DEV * nb, 9 * dm)
    dmod_cols = lax.dynamic_slice_in_dim(dmod_all, chip * ncol, ncol, axis=1)
    grads["w_ada"] = _ada_bwd(c_all, dmod_cols, name="ada_bwd")

    delta, new_m, new_v = {}, {}, {}
    as2d = lambda a: a.reshape(POOL_G * LANE, LANE) if a.ndim == 4 else a.reshape(1, -1)
    upd, loss_row = _small_update(red, dmod_rows, pool_total, nb,
                                  {n: tuple(as2d(args[p + n]) for p in ("", "m_", "v_")) for n in SMALL}, name="small_update")
    loss = loss_row[0, 0]
    for n in SMALL:
        grads[n], delta[n], new_m[n], new_v[n] = upd[n]
    for n in ("w_ada",) + BIG:
        if n in NARROW:
            res = _adamw(wts[n].T, grads_t[n] if n in TRANSPOSED else grads[n].T, mom[n].T, var[n].T, name="adamw_" + n)
            delta[n], new_m[n], new_v[n] = (r.T for r in res)
        else:
            delta[n], new_m[n], new_v[n] = _adamw(wts[n], grads[n], mom[n], var[n], name="adamw_" + n)

    def lead(a, n):
        return a.reshape((1,) + wts[n].shape)

    return (loss, grad_x, *[lead(grads[n], n) for n in WEIGHTS], *[lead(delta[n], n) for n in WEIGHTS],
            *[lead(new_m[n], n) for n in WEIGHTS], *[lead(new_v[n], n) for n in WEIGHTS])
```

```python
import functools
import math

import jax
import jax.numpy as jnp
from jax import lax
from jax.experimental import pallas as pl
from jax.experimental.pallas import tpu as pltpu

F32 = jnp.float32
MX = jnp.bfloat16

D = 1024
DFF = 2816
NH = 8
POOL_W = 512
POOL_G = 4
QL = 384
KVL = 256
ROPE = 32
NOPE = 64
LANE = 128
SUBLANE = 8
EPS = 1e-6
ATTN_SCALE = 1.0 / math.sqrt(96.0)
NEG = -1e30

Z_UP, Z_QL, Z_KV, Z_KR, Z_GP, Z_GM, Z_W = 0, 512, 896, 1152, 1280, 2304, 3328
KR_LANE = 64
LAT_W = 1280

ADAM_LR, ADAM_B1, ADAM_B2, ADAM_EPS, ADAM_WD, ADAM_STEP = 0.001, 0.9, 0.999, 1e-08, 0.01, 10

VMEM_LIMIT = 48 * 1024 * 1024
MESH = pl.DeviceIdType.MESH
N_DEV = 8
N_CHIP = 4


class _Comm:
    def __init__(self, ins, out_shapes, sems, start, finish, aliases=None):
        self.ins, self.out_shapes, self.sems = list(ins), list(out_shapes), list(sems)
        self.start, self.finish = start, finish
        self.aliases = aliases or {}


def _pcall(body, *, name, out_shape, grid=(), in_specs=None, out_specs=None, scratch=(), grid_spec=None, aliases=None,
           comm=None):
    params = pltpu.CompilerParams(vmem_limit_bytes=VMEM_LIMIT)
    kw = dict(name=name, compiler_params=params)
    if comm is None:
        if aliases:
            kw["input_output_aliases"] = aliases
        if grid_spec is not None:
            return pl.pallas_call(body, grid_spec=grid_spec, out_shape=out_shape, **kw)
        return pl.pallas_call(body, grid=grid, in_specs=in_specs, out_specs=out_specs, scratch_shapes=scratch,
                              out_shape=out_shape, **kw)
    single = not isinstance(out_shape, (list, tuple))
    outs = [out_shape] if single else list(out_shape)
    ospecs = [out_specs] if single else list(out_specs)
    n_in, n_out, n_ci, n_co, n_scr = len(in_specs), len(outs), len(comm.ins), len(comm.out_shapes), len(scratch)
    io = dict(aliases or {})
    io.update({n_in + i: n_out + o for i, o in comm.aliases.items()})

    def riding(*refs):
        ins, cins = refs[:n_in], refs[n_in:n_in + n_ci]
        at = n_in + n_ci
        os_, couts = refs[at:at + n_out], refs[at + n_out:at + n_out + n_co]
        at += n_out + n_co
        scr, csems = refs[at:at + n_scr], refs[at + n_scr:]
        if grid:
            first = functools.reduce(jnp.logical_and, [pl.program_id(d) == 0 for d in range(len(grid))])
            last = functools.reduce(jnp.logical_and, [pl.program_id(d) == grid[d] - 1 for d in range(len(grid))])
            pl.when(first)(lambda: comm.start(cins, couts, csems))
            body(*ins, *os_, *scr)
            pl.when(last)(lambda: comm.finish(cins, couts, csems))
        else:
            comm.start(cins, couts, csems)
            body(*ins, *os_, *scr)
            comm.finish(cins, couts, csems)

    call = pl.pallas_call(riding, grid=grid, in_specs=list(in_specs) + [HBM_SPEC] * n_ci, out_specs=ospecs + [HBM_SPEC] * n_co,
                          out_shape=outs + comm.out_shapes, scratch_shapes=list(scratch) + comm.sems,
                          input_output_aliases=io, **kw)

    def run(*args):
        res = call(*args, *comm.ins)
        main = list(res[:n_out])
        return (main[0] if single else main), list(res[n_out:])

    return run


def _pick(dim, target):
    best = None
    for t in range(LANE, min(dim, target) + 1, LANE):
        if dim % t == 0:
            best = t
    return dim if best is None else best


def _sds(shape, dtype):
    return jax.ShapeDtypeStruct(shape, dtype)


def _dw(a, g, *, name, tm=512, tn=1024, out_t=False, comm=None):
    return _mm(a, g, mode="tn", out_dtype=F32, name=name, tm=tm, tn=tn, tk=a.shape[0], n_outer=True, out_t=out_t, comm=comm)


def _mm(a, b, *, mode, out_dtype, name, tm=1024, tn=1024, tk=4096, n_outer=False, out_t=False, comm=None):
    if mode == "nn":
        (M, K), (K2, N) = a.shape, b.shape
    elif mode == "nt":
        (M, K), (N, K2) = a.shape, b.shape
    else:
        (K, M), (K2, N) = a.shape, b.shape
    assert K == K2, (name, a.shape, b.shape)
    tm, tn, tk = _pick(M, tm), _pick(N, tn), _pick(K, tk)
    nk = K // tk
    if n_outer:
        ij = lambda g0, g1: (g1, g0)
        grid = (N // tn, M // tm, nk)
    else:
        ij = lambda g0, g1: (g0, g1)
        grid = (M // tm, N // tn, nk)
    if mode == "tn":
        a_spec = pl.BlockSpec((tk, tm), lambda g0, g1, k: (k, ij(g0, g1)[0]))
    else:
        a_spec = pl.BlockSpec((tm, tk), lambda g0, g1, k: (ij(g0, g1)[0], k))
    if mode == "nt":
        b_spec = pl.BlockSpec((tn, tk), lambda g0, g1, k: (ij(g0, g1)[1], k))
    else:
        b_spec = pl.BlockSpec((tk, tn), lambda g0, g1, k: (k, ij(g0, g1)[1]))
    if out_t:
        assert nk == 1, name
        o_spec = pl.BlockSpec((tn, tm), lambda g0, g1, k: ij(g0, g1)[::-1])
    else:
        o_spec = pl.BlockSpec((tm, tn), lambda g0, g1, k: ij(g0, g1))
    dn = {"nn": (((1,), (0,)), ((), ())), "nt": (((1,), (1,)), ((), ())), "tn": (((0,), (0,)), ((), ()))}[mode]

    def dot(a_ref, b_ref):
        return lax.dot_general(a_ref[...].astype(MX), b_ref[...].astype(MX), dn, preferred_element_type=F32)

    def body_one(a_ref, b_ref, o_ref):
        prod = dot(a_ref, b_ref)
        o_ref[...] = (prod.T if out_t else prod).astype(o_ref.dtype)

    def body_acc(a_ref, b_ref, o_ref, acc_ref):
        k = pl.program_id(2)
        part = dot(a_ref, b_ref)

        @pl.when(k == 0)
        def _():
            acc_ref[...] = part

        @pl.when(k > 0)
        def _():
            acc_ref[...] += part

        @pl.when(k == nk - 1)
        def _():
            o_ref[...] = acc_ref[...].astype(o_ref.dtype)

    return _pcall(body_one if nk == 1 else body_acc, name=name, out_shape=_sds((N, M) if out_t else (M, N), out_dtype), grid=grid,
                  in_specs=[a_spec, b_spec], out_specs=o_spec, scratch=[] if nk == 1 else [pltpu.VMEM((tm, tn), F32)],
                  comm=comm)(a, b)


def _gu_shard(q):
    return (q % 2) * 2 + q // 2


def _ffn_up(h, w_st, *, name, tm=512, comm=None):
    T, dm = h.shape
    hw = w_st.shape[2]
    tm = _pick(T, tm)

    def body(h_ref, wg_ref, wu_ref, gu_ref, a_ref):
        hv = h_ref[...]
        g = jnp.dot(hv, wg_ref[0], preferred_element_type=F32)
        u = jnp.dot(hv, wu_ref[0], preferred_element_type=F32)
        gu_ref[:, :hw] = g.astype(gu_ref.dtype)
        gu_ref[:, hw:] = u.astype(gu_ref.dtype)
        a_ref[...] = (g * _sigmoid(g) * u).astype(a_ref.dtype)

    return _pcall(body, name=name, grid=(T // tm, 2),
                  in_specs=[pl.BlockSpec((tm, dm), lambda i, j: (i, 0)), pl.BlockSpec((1, dm, hw), lambda i, j: (j, 0, 0)),
                            pl.BlockSpec((1, dm, hw), lambda i, j: (2 + j, 0, 0))],
                  out_specs=[pl.BlockSpec((tm, 2 * hw), lambda i, j: (i, j)), pl.BlockSpec((tm, hw), lambda i, j: (i, j))],
                  out_shape=[_sds((T, 4 * hw), MX), _sds((T, 2 * hw), MX)], comm=comm)(h, w_st, w_st)


def _ffn_up_first(x, mod3, gain, w_st, *, sub, name, tm=512, comm=None):
    T, dm = x.shape
    hw = w_st.shape[2]
    tm = _pick(T, tm)
    tps = (T // mod3.shape[0]) // tm

    def body(x_ref, mod_ref, n_ref, wg_ref, wu_ref, h_ref, gu_ref, a_ref):
        xv = x_ref[...]
        xn = xv * _rsq(xv) * n_ref[...]
        hv = (xn * (1.0 + mod_ref[0, 3 * sub + 1:3 * sub + 2, :]) + mod_ref[0, 3 * sub:3 * sub + 1, :]).astype(h_ref.dtype)
        h_ref[...] = hv
        g = jnp.dot(hv, wg_ref[0], preferred_element_type=F32)
        u = jnp.dot(hv, wu_ref[0], preferred_element_type=F32)
        gu_ref[:, :hw] = g.astype(gu_ref.dtype)
        gu_ref[:, hw:] = u.astype(gu_ref.dtype)
        a_ref[...] = (g * _sigmoid(g) * u).astype(a_ref.dtype)

    return _pcall(body, name=name, grid=(T // tm,),
                  in_specs=[_row_spec(tm, dm), pl.BlockSpec((1, 9, dm), lambda i: (i // tps, 0, 0)),
                            pl.BlockSpec((1, dm), lambda i: (0, 0)), pl.BlockSpec((1, dm, hw), lambda i: (0, 0, 0)),
                            pl.BlockSpec((1, dm, hw), lambda i: (2, 0, 0))],
                  out_specs=[_row_spec(tm, dm), pl.BlockSpec((tm, 2 * hw), lambda i: (i, 0)), pl.BlockSpec((tm, hw), lambda i: (i, 0))],
                  out_shape=[_sds((T, dm), MX), _sds((T, 4 * hw), MX), _sds((T, 2 * hw), MX)],
                  comm=comm)(x, mod3, gain, w_st, w_st)


def _ffn_up_second(h, w_st, gu, a, *, name, tm=512, comm=None):
    T, dm = h.shape
    hw = w_st.shape[2]
    tm = _pick(T, tm)

    def body(h_ref, wg_ref, wu_ref, gu_in, a_in, gu_ref, a_ref):
        hv = h_ref[...]
        g = jnp.dot(hv, wg_ref[0], preferred_element_type=F32)
        u = jnp.dot(hv, wu_ref[0], preferred_element_type=F32)
        gu_ref[:, :hw] = g.astype(gu_ref.dtype)
        gu_ref[:, hw:] = u.astype(gu_ref.dtype)
        a_ref[...] = (g * _sigmoid(g) * u).astype(a_ref.dtype)

    return _pcall(body, name=name, grid=(T // tm,),
                  in_specs=[_row_spec(tm, dm), pl.BlockSpec((1, dm, hw), lambda i: (1, 0, 0)),
                            pl.BlockSpec((1, dm, hw), lambda i: (3, 0, 0)), HBM_SPEC, HBM_SPEC],
                  out_specs=[pl.BlockSpec((tm, 2 * hw), lambda i: (i, 1)), pl.BlockSpec((tm, hw), lambda i: (i, 1))],
                  out_shape=[_sds(gu.shape, gu.dtype), _sds(a.shape, a.dtype)], aliases={3: 0, 4: 1},
                  comm=comm)(h, w_st, w_st, gu, a)


def _ffn_dact(df, gu, w_out, *, name, tm=512, comm=None):
    T, dm = df.shape
    hw = gu.shape[1] // 4
    tm = _pick(T, tm)

    def body(df_ref, gu_ref, wo_ref, dgu_ref):
        da = lax.dot_general(df_ref[...], wo_ref[...], (((1,), (1,)), ((), ())), preferred_element_type=F32)
        g = gu_ref[:, :hw].astype(F32)
        u = gu_ref[:, hw:].astype(F32)
        s = _sigmoid(g)
        dgu_ref[:, :hw] = (da * u * (s * (1.0 + g * (1.0 - s)))).astype(dgu_ref.dtype)
        dgu_ref[:, hw:] = (da * (g * s)).astype(dgu_ref.dtype)

    return _pcall(body, name=name, grid=(T // tm, 2),
                  in_specs=[pl.BlockSpec((tm, dm), lambda i, j: (i, 0)), pl.BlockSpec((tm, 2 * hw), lambda i, j: (i, j)),
                            pl.BlockSpec((hw, dm), lambda i, j: (j, 0))],
                  out_specs=pl.BlockSpec((tm, 2 * hw), lambda i, j: (i, j)), out_shape=_sds(gu.shape, MX),
                  comm=comm)(df, gu, w_out)


def _ffn_dh(dgu, w_st, *, name, tm=1024, comm=None):
    T = dgu.shape[0]
    _, dm, hw = w_st.shape
    tm = _pick(T, tm)

    def body(d_ref, w_ref, o_ref, acc_ref):
        q = pl.program_id(1)
        part = lax.dot_general(d_ref[...], w_ref[0], (((1,), (1,)), ((), ())), preferred_element_type=F32)

        @pl.when(q == 0)
        def _():
            acc_ref[...] = part

        @pl.when(jnp.logical_and(q > 0, q < 3))
        def _():
            acc_ref[...] += part

        @pl.when(q == 3)
        def _():
            o_ref[...] = acc_ref[...] + part

    return _pcall(body, name=name, grid=(T // tm, 4),
                  in_specs=[pl.BlockSpec((tm, hw), lambda i, q: (i, q)), pl.BlockSpec((1, dm, hw), lambda i, q: (_gu_shard(q), 0, 0))],
                  out_specs=pl.BlockSpec((tm, dm), lambda i, q: (i, 0)), out_shape=_sds((T, dm), F32),
                  scratch=[pltpu.VMEM((tm, dm), F32)], comm=comm)(dgu, w_st)


def _ffn_dw_in(h, dgu, *, name, rows=None, tm=512, comm=None):
    T, dm = h.shape
    hw = dgu.shape[1] // 4
    first, count = rows if rows is not None else (0, dm)
    tm = _pick(count, tm)
    skip = first // tm

    def body(h_ref, d_ref, o_ref):
        o_ref[0] = lax.dot_general(h_ref[...], d_ref[...], (((0,), (0,)), ((), ())), preferred_element_type=F32)

    return _pcall(body, name=name, grid=(4, count // tm),
                  in_specs=[pl.BlockSpec((T, tm), lambda q, i: (0, skip + i)), pl.BlockSpec((T, hw), lambda q, i: (0, q))],
                  out_specs=pl.BlockSpec((1, tm, hw), lambda q, i: (_gu_shard(q), i, 0)),
                  out_shape=_sds((4, count, hw), F32), comm=comm)(h, dgu)


def _rsq(x):
    return lax.rsqrt(jnp.mean(x * x, axis=-1, keepdims=True) + EPS)


def _sigmoid(x):
    return 1.0 / (1.0 + jnp.exp(-x))


def _row_spec(tm, w):
    return pl.BlockSpec((tm, w), lambda i: (i, 0))


def _mm_resid_norm(a, w, x_prev, mod3, gain, *, sub, coef, name, tm=512, comm=None):
    T, k = a.shape
    dm = w.shape[1]
    tm = _pick(T, tm)
    tps = (T // mod3.shape[0]) // tm

    def body(a_ref, w_ref, x_ref, mod_ref, n_ref, f_ref, xo_ref, h_ref):
        f = jnp.dot(a_ref[...], w_ref[...], preferred_element_type=F32)
        f_ref[...] = f
        x = x_ref[...] + coef * mod_ref[0, 3 * sub - 1:3 * sub, :] * f
        xo_ref[...] = x
        xn = x * _rsq(x) * n_ref[...]
        h_ref[...] = (xn * (1.0 + mod_ref[0, 3 * sub + 1:3 * sub + 2, :]) + mod_ref[0, 3 * sub:3 * sub + 1, :]).astype(h_ref.dtype)

    row = _row_spec(tm, dm)
    return _pcall(body, name=name, grid=(T // tm,),
                  in_specs=[_row_spec(tm, k), pl.BlockSpec((k, dm), lambda i: (0, 0)), row,
                            pl.BlockSpec((1, 9, dm), lambda i: (i // tps, 0, 0)), pl.BlockSpec((1, dm), lambda i: (0, 0))],
                  out_specs=[row, row, row], out_shape=[_sds((T, dm), F32), _sds((T, dm), F32), _sds((T, dm), MX)],
                  comm=comm)(a, w, x_prev, mod3, gain)


def _mm_loss(a, w, x2, tgt, mod3, *, name, tm=512):
    T, k = a.shape
    dm = w.shape[1]
    tm = _pick(T, tm)
    tps = (T // mod3.shape[0]) // tm
    mod_spec = pl.BlockSpec((1, 9, dm), lambda i: (i // tps, 0, 0))
    row = _row_spec(tm, dm)
    stat_spec = pl.BlockSpec((1, SUBLANE, dm), lambda i: (i // tps, 0, 0))
    loss_spec = pl.BlockSpec((SUBLANE, LANE), lambda i: (0, 0))

    def body(a_ref, w_ref, x_ref, t_ref, mod_ref, dy_ref, df_ref, st_ref, loss_ref):
        i = pl.program_id(0)
        g = mod_ref[0, 8:9, :]
        f = jnp.dot(a_ref[...], w_ref[...], preferred_element_type=F32)
        err = x_ref[...] + 0.5 * g * f - t_ref[...]
        dy = err * (1.0 / dm)
        dy_ref[...] = dy
        df_ref[...] = (0.5 * g * dy).astype(df_ref.dtype)
        dgate = jnp.sum(0.5 * dy * f, axis=0, keepdims=True)
        part = 0.5 * jnp.sum(jnp.sum(err * err, axis=0, keepdims=True), axis=1, keepdims=True) * (1.0 / dm)

        @pl.when(i % tps == 0)
        def _():
            st_ref[...] = jnp.zeros_like(st_ref)

        @pl.when(i == 0)
        def _():
            loss_ref[...] = jnp.zeros_like(loss_ref)

        st_ref[0, 0:1, :] += dgate
        loss_ref[...] += jnp.broadcast_to(part, loss_ref.shape)

    return _pcall(body, name=name, grid=(T // tm,),
                  in_specs=[_row_spec(tm, k), pl.BlockSpec((k, dm), lambda i: (0, 0)), row, row, mod_spec],
                  out_specs=[row, row, stat_spec, loss_spec],
                  out_shape=[_sds((T, dm), F32), _sds((T, dm), MX), _sds((mod3.shape[0], SUBLANE, dm), F32),
                             _sds((SUBLANE, LANE), F32)])(a, w, x2, tgt, mod3)


def _norm_bwd_tail(dhv, x_ref, dxi_ref, f_ref, mod_ref, n_ref, dx_ref, df_ref, st_ref, *, first, sub, coef):
    x = x_ref[...]
    r = _rsq(x)
    xhat = x * r
    n = n_ref[...]
    d_shift = jnp.sum(dhv, axis=0, keepdims=True)
    d_scale = jnp.sum(dhv * (xhat * n), axis=0, keepdims=True)
    dxn = dhv * (1.0 + mod_ref[0, 3 * sub + 1:3 * sub + 2, :])
    d_gain = jnp.sum(dxn * xhat, axis=0, keepdims=True)
    dxhat = dxn * n
    dx = dxi_ref[...] + r * (dxhat - xhat * jnp.mean(dxhat * xhat, axis=-1, keepdims=True))
    dx_ref[...] = dx

    @pl.when(first)
    def _():
        st_ref[...] = jnp.zeros_like(st_ref)

    st_ref[0, 0:1, :] += d_shift
    st_ref[0, 1:2, :] += d_scale
    st_ref[0, 2:3, :] += d_gain
    if f_ref is not None:
        st_ref[0, 3:4, :] += jnp.sum(coef * dx * f_ref[...], axis=0, keepdims=True)
        df_ref[...] = (coef * mod_ref[0, 3 * sub - 1:3 * sub, :] * dx).astype(df_ref.dtype)


def _ffn_dh_norm(dgu, w_st, x_cur, dx_in, f_prev, mod3, gain, *, sub, coef, name, tm=512, comm=None):
    T = dgu.shape[0]
    _, dm, hw = w_st.shape
    tm = _pick(T, tm)
    nb = mod3.shape[0]
    tps = (T // nb) // tm

    def body(d_ref, w_ref, x_ref, dxi_ref, f_ref, mod_ref, n_ref, dx_ref, df_ref, st_ref, acc_ref):
        i = pl.program_id(0)
        q = pl.program_id(1)
        part = lax.dot_general(d_ref[...], w_ref[0], (((1,), (1,)), ((), ())), preferred_element_type=F32)

        @pl.when(q == 0)
        def _():
            acc_ref[...] = part

        @pl.when(q > 0)
        def _():
            acc_ref[...] += part

        @pl.when(q == 3)
        def _():
            _norm_bwd_tail(acc_ref[...], x_ref, dxi_ref, f_ref, mod_ref, n_ref, dx_ref, df_ref, st_ref,
                           first=i % tps == 0, sub=sub, coef=coef)

    row = pl.BlockSpec((tm, dm), lambda i, q: (i, 0))
    return _pcall(body, name=name, grid=(T // tm, 4),
                  in_specs=[pl.BlockSpec((tm, hw), lambda i, q: (i, q)), pl.BlockSpec((1, dm, hw), lambda i, q: (_gu_shard(q), 0, 0)),
                            row, row, row, pl.BlockSpec((1, 9, dm), lambda i, q: (i // tps, 0, 0)),
                            pl.BlockSpec((1, dm), lambda i, q: (0, 0))],
                  out_specs=[row, row, pl.BlockSpec((1, SUBLANE, dm), lambda i, q: (i // tps, 0, 0))],
                  out_shape=[_sds((T, dm), F32), _sds((T, dm), MX), _sds((nb, SUBLANE, dm), F32)],
                  scratch=[pltpu.VMEM((tm, dm), F32)], comm=comm)(dgu, w_st, x_cur, dx_in, f_prev, mod3, gain)


def _mm_norm_bwd(a, b, x_cur, dx_in, f_prev, mod3, gain, *, sub, coef, name, tm=512, comm=None):
    T, k = a.shape
    dm = b.shape[1]
    tm = _pick(T, tm)
    nb = mod3.shape[0]
    tps = (T // nb) // tm

    def body(a_ref, b_ref, x_ref, dxi_ref, f_ref, mod_ref, n_ref, dx_ref, df_ref, st_ref):
        dh = jnp.dot(a_ref[...], b_ref[...], preferred_element_type=F32)
        _norm_bwd_tail(dh, x_ref, dxi_ref, f_ref, mod_ref, n_ref, dx_ref, df_ref, st_ref,
                       first=pl.program_id(0) % tps == 0, sub=sub, coef=coef)

    row = _row_spec(tm, dm)
    return _pcall(body, name=name, grid=(T // tm,),
                  in_specs=[_row_spec(tm, k), pl.BlockSpec((k, dm), lambda i: (0, 0)), row, row, row,
                            pl.BlockSpec((1, 9, dm), lambda i: (i // tps, 0, 0)), pl.BlockSpec((1, dm), lambda i: (0, 0))],
                  out_specs=[row, row, pl.BlockSpec((1, SUBLANE, dm), lambda i: (i // tps, 0, 0))],
                  out_shape=[_sds((T, dm), F32), _sds((T, dm), MX), _sds((nb, SUBLANE, dm), F32)],
                  comm=comm)(a, b, x_cur, dx_in, f_prev, mod3, gain)


def _bwd_block(x_cur, dh, dx_in, f_prev, mod3, gain, *, sub, coef, name, tm=256, comm=None):
    T, dm = x_cur.shape
    nb = mod3.shape[0]
    tps = (T // nb) // tm
    has_f = f_prev is not None
    mod_spec = pl.BlockSpec((1, 9, dm), lambda i: (i // tps, 0, 0))
    vec_spec = pl.BlockSpec((1, dm), lambda i: (0, 0))
    stat_spec = pl.BlockSpec((1, SUBLANE, dm), lambda i: (i // tps, 0, 0))
    row = _row_spec(tm, dm)

    def body(*refs):
        if has_f:
            x_ref, dh_ref, dxi_ref, f_ref, mod_ref, n_ref, dx_ref, df_ref, st_ref = refs
        else:
            x_ref, dh_ref, dxi_ref, mod_ref, n_ref, dx_ref, st_ref = refs
            f_ref = df_ref = None
        _norm_bwd_tail(dh_ref[...], x_ref, dxi_ref, f_ref, mod_ref, n_ref, dx_ref, df_ref, st_ref,
                       first=pl.program_id(0) % tps == 0, sub=sub, coef=coef)

    st_shape = _sds((nb, SUBLANE, dm), F32)
    if has_f:
        return _pcall(body, name=name, grid=(T // tm,), in_specs=[row, row, row, row, mod_spec, vec_spec],
                      out_specs=[row, row, stat_spec],
                      out_shape=[_sds((T, dm), F32), _sds((T, dm), MX), st_shape], comm=comm)(x_cur, dh, dx_in, f_prev, mod3, gain)
    return _pcall(body, name=name, grid=(T // tm,), in_specs=[row, row, row, mod_spec, vec_spec],
                  out_specs=[row, stat_spec], out_shape=[_sds((T, dm), F32), st_shape], aliases={0: 0},
                  comm=comm)(x_cur, dh, dx_in, mod3, gain)


def _mix_out_fwd(z, br_pool, attn, w_mla, w_out, x_prev, mod3, gain, *, sub, name, tm=512):
    T = z.shape[0]
    dm = w_out.shape[1]
    tm = _pick(T, tm)
    tps = (T // mod3.shape[0]) // tm
    whole = lambda a: pl.BlockSpec(a.shape, lambda i: (0, 0))

    def body(z_ref, bp_ref, at_ref, wm_ref, wo_ref, x_ref, mod_ref, n_ref, bm_ref, mg_ref, mo_ref, xo_ref, h_ref):
        bm = jnp.dot(at_ref[...], wm_ref[...], preferred_element_type=F32).astype(MX)
        bm_ref[...] = bm
        gp = z_ref[:, Z_GP:Z_GP + D].astype(F32)
        gm = z_ref[:, Z_GM:Z_GM + D].astype(F32)
        merged = (_sigmoid(gp) * bp_ref[...] + _sigmoid(gm) * bm).astype(MX)
        mg_ref[...] = merged
        mo = jnp.dot(merged, wo_ref[...], preferred_element_type=F32)
        mo_ref[...] = mo
        x = x_ref[...] + mod_ref[0, 3 * sub - 1:3 * sub, :] * mo
        xo_ref[...] = x
        xn = x * _rsq(x) * n_ref[...]
        h_ref[...] = (xn * (1.0 + mod_ref[0, 3 * sub + 1:3 * sub + 2, :]) + mod_ref[0, 3 * sub:3 * sub + 1, :]).astype(h_ref.dtype)

    row = _row_spec(tm, dm)
    return _pcall(body, name=name, grid=(T // tm,),
                  in_specs=[_row_spec(tm, Z_W), row, _row_spec(tm, attn.shape[1]), whole(w_mla), whole(w_out), row,
                            pl.BlockSpec((1, 9, dm), lambda i: (i // tps, 0, 0)), pl.BlockSpec((1, dm), lambda i: (0, 0))],
                  out_specs=[row, row, row, row, row],
                  out_shape=[_sds((T, dm), MX), _sds((T, dm), MX), _sds((T, dm), F32), _sds((T, dm), F32), _sds((T, dm), MX)],
                  )(z, br_pool, attn, w_mla, w_out, x_prev, mod3, gain)


def _mix_out_bwd(dmo, w_out, z, br_pool, br_mla, w_mla, *, name, tm=512):
    T = z.shape[0]
    tm = _pick(T, tm)
    whole = lambda a: pl.BlockSpec(a.shape, lambda i: (0, 0))
    nt = (((1,), (1,)), ((), ()))

    def body(dmo_ref, wo_ref, z_ref, bp_ref, bm_ref, wm_ref, dbp_ref, dbm_ref, dg_ref, dat_ref):
        dm = lax.dot_general(dmo_ref[...], wo_ref[...], nt, preferred_element_type=F32)
        sp = _sigmoid(z_ref[:, Z_GP:Z_GP + D].astype(F32))
        sm = _sigmoid(z_ref[:, Z_GM:Z_GM + D].astype(F32))
        dbp_ref[...] = (dm * sp).astype(dbp_ref.dtype)
        dbm = (dm * sm).astype(MX)
        dbm_ref[...] = dbm
        dg_ref[:, :D] = (dm * bp_ref[...].astype(F32) * sp * (1.0 - sp)).astype(dg_ref.dtype)
        dg_ref[:, D:] = (dm * bm_ref[...].astype(F32) * sm * (1.0 - sm)).astype(dg_ref.dtype)
        dat_ref[...] = lax.dot_general(dbm, wm_ref[...], nt, preferred_element_type=F32)

    row = _row_spec(tm, D)
    return _pcall(body, name=name, grid=(T // tm,),
                  in_specs=[row, whole(w_out), _row_spec(tm, Z_W), row, row, whole(w_mla)],
                  out_specs=[row, row, _row_spec(tm, 2 * D), _row_spec(tm, w_mla.shape[0])],
                  out_shape=[_sds((T, D), MX), _sds((T, D), MX), _sds((T, 2 * D), MX), _sds((T, w_mla.shape[0]), F32)],
                  )(dmo, w_out, z, br_pool, br_mla, w_mla)


def _shift_down(x, k, row):
    return jnp.where(row >= k, pltpu.roll(x, k, 0), 0.0)


def _shift_up(x, k, row, n):
    return jnp.where(row < n - k, pltpu.roll(x, n - k, 0), 0.0)


def _pool_fwd(z, pool_grp, pool_scale, *, nb, name):
    T = z.shape[0]
    S = T // nb
    blk = pl.BlockSpec((S, LANE), lambda b, g: (b, g))

    def body(u_ref, w_ref, s_ref, pooled_ref, mixed_ref, scaled_ref):
        g = pl.program_id(1)
        u = u_ref[...].astype(F32)
        row = lax.broadcasted_iota(jnp.int32, u.shape, 0)
        s2 = u + _shift_down(u, 1, row)
        s4 = s2 + _shift_down(s2, 2, row)
        s8 = s4 + _shift_down(s4, 4, row)
        s16 = s8 + _shift_down(s8, 8, row)
        win = jnp.where(g == 0, s2, jnp.where(g == 1, s4, jnp.where(g == 2, s8, s16)))
        width = lax.shift_left(jnp.int32(2), g)
        cnt = jnp.minimum(row + 1, width).astype(F32)
        pooled = (win / cnt - u).astype(MX)
        pooled_ref[...] = pooled
        mixed = jnp.dot(pooled, w_ref[0], preferred_element_type=F32)
        mixed_ref[...] = mixed
        scaled_ref[...] = (mixed * s_ref[...]).astype(scaled_ref.dtype)

    return _pcall(body, name=name, grid=(nb, POOL_G),
                  in_specs=[blk, pl.BlockSpec((1, LANE, LANE), lambda b, g: (g, 0, 0)),
                            pl.BlockSpec((1, LANE), lambda b, g: (0, g))],
                  out_specs=[blk, blk, blk],
                  out_shape=[_sds((T, POOL_W), MX), _sds((T, POOL_W), F32), _sds((T, POOL_W), MX)])(z, pool_grp, pool_scale)


def _pool_bwd(dscaled, mixed, pooled, pool_grp, pool_scale, *, nb, name):
    T = dscaled.shape[0]
    S = T // nb
    blk = pl.BlockSpec((S, LANE), lambda g, b: (b, g))

    def body(ds_ref, mixed_ref, pooled_ref, w_ref, s_ref, du_ref, dw_ref, dsc_ref):
        g = pl.program_id(0)
        b = pl.program_id(1)
        ds = ds_ref[...]
        dsc_ref[0] = jnp.sum(ds * mixed_ref[...], axis=0, keepdims=True)
        dmixed = (ds * s_ref[...]).astype(MX)
        dw = lax.dot_general(pooled_ref[...], dmixed, (((0,), (0,)), ((), ())), preferred_element_type=F32)

        @pl.when(b == 0)
        def _():
            dw_ref[0] = dw

        @pl.when(b > 0)
        def _():
            dw_ref[0] += dw
        dpooled = lax.dot_general(dmixed, w_ref[0], (((1,), (1,)), ((), ())), preferred_element_type=F32)
        row = lax.broadcasted_iota(jnp.int32, dpooled.shape, 0)
        width = lax.shift_left(jnp.int32(2), g)
        q = dpooled / jnp.minimum(row + 1, width).astype(F32)
        r2 = q + _shift_up(q, 1, row, S)
        r4 = r2 + _shift_up(r2, 2, row, S)
        r8 = r4 + _shift_up(r4, 4, row, S)
        r16 = r8 + _shift_up(r8, 8, row, S)
        win = jnp.where(g == 0, r2, jnp.where(g == 1, r4, jnp.where(g == 2, r8, r16)))
        du_ref[...] = (win - dpooled).astype(du_ref.dtype)

    return _pcall(body, name=name, grid=(POOL_G, nb),
                  in_specs=[blk, blk, blk, pl.BlockSpec((1, LANE, LANE), lambda g, b: (g, 0, 0)),
                            pl.BlockSpec((1, LANE), lambda g, b: (0, g))],
                  out_specs=[blk, pl.BlockSpec((1, LANE, LANE), lambda g, b: (g, 0, 0)),
                             pl.BlockSpec((1, 1, LANE), lambda g, b: (b, 0, g))],
                  out_shape=[_sds((T, POOL_W), MX), _sds((POOL_G, LANE, LANE), F32), _sds((nb, 1, POOL_W), F32)],
                  )(dscaled, mixed, pooled, pool_grp, pool_scale)


def _lane_masks(shape):
    lane = lax.broadcasted_iota(jnp.int32, shape, len(shape) - 1)
    m_n = lane < NOPE
    m_r = jnp.logical_and(lane >= KR_LANE, lane < KR_LANE + ROPE)
    first_half = lane < KR_LANE + ROPE // 2
    return m_n, m_r, first_half


def _rot(y, first_half):
    return jnp.where(first_half, -pltpu.roll(y, LANE - ROPE // 2, 1), pltpu.roll(y, ROPE // 2, 1))


def _rot_t(v, first_half, m_r):
    return jnp.where(m_r, jnp.where(first_half, pltpu.roll(v, LANE - ROPE // 2, 1), -pltpu.roll(v, ROPE // 2, 1)), 0.0)


def _mla_in_fwd(z, w_q, w_kv, qa_gain, kva_gain, cos, sin, q_gain, k_gain, *, name, tm=256):
    T = z.shape[0]
    slab = pl.BlockSpec((tm, LANE), lambda i: (i, 0))
    vec = pl.BlockSpec((1, LANE), lambda i: (0, 0))
    whole = lambda a: pl.BlockSpec(a.shape, lambda i: (0, 0))

    def body(z_ref, wq_ref, wkv_ref, qa_ref, kva_ref, cos_ref, sin_ref, qg_ref, kg_ref,
             qn_ref, kvn_ref, qp_ref, kvp_ref, q_ref, k_ref, v_ref):
        ql = z_ref[:, Z_QL:Z_QL + QL].astype(F32)
        kvl = z_ref[:, Z_KV:Z_KV + KVL].astype(F32)
        qn = (ql * _rsq(ql) * qa_ref[...]).astype(MX)
        kvn = (kvl * _rsq(kvl) * kva_ref[...]).astype(MX)
        qn_ref[...] = qn
        kvn_ref[...] = kvn
        qp = jnp.dot(qn, wq_ref[...], preferred_element_type=F32)
        kvp = jnp.dot(kvn, wkv_ref[...], preferred_element_type=F32)
        qp_ref[...] = qp
        kvp_ref[...] = kvp
        m_n, m_r, first_half = _lane_masks((tm, LANE))
        c = cos_ref[...]
        s = sin_ref[...]
        qg = qg_ref[...]
        kg = kg_ref[...]
        xr = z_ref[:, Z_KR:Z_KR + LANE].astype(F32)
        rr = lax.rsqrt(jnp.sum(xr * xr, axis=-1, keepdims=True) * (1.0 / ROPE) + EPS)
        yr = xr * rr * kg
        kr = jnp.where(m_r, yr * c + _rot(yr, first_half) * s, 0.0)
        for h in range(NH):
            x = qp[:, h * LANE:(h + 1) * LANE]
            x2 = x * x
            rn = lax.rsqrt(jnp.sum(jnp.where(m_n, x2, 0.0), axis=-1, keepdims=True) * (1.0 / NOPE) + EPS)
            rq = lax.rsqrt(jnp.sum(jnp.where(m_r, x2, 0.0), axis=-1, keepdims=True) * (1.0 / ROPE) + EPS)
            y = x * jnp.where(m_n, rn, jnp.where(m_r, rq, 0.0)) * qg
            q_ref[:, h * LANE:(h + 1) * LANE] = ((y * c + _rot(y, first_half) * s) * ATTN_SCALE).astype(q_ref.dtype)
            xk = kvp[:, h * LANE:(h + 1) * LANE]
            rk = lax.rsqrt(jnp.sum(jnp.where(m_n, xk * xk, 0.0), axis=-1, keepdims=True) * (1.0 / NOPE) + EPS)
            k_ref[:, h * LANE:(h + 1) * LANE] = (jnp.where(m_n, xk * rk * kg, 0.0) + kr).astype(k_ref.dtype)
        v_ref[...] = kvp[:, NH * LANE:].astype(v_ref.dtype)

    return _pcall(body, name=name, grid=(T // tm,),
                  in_specs=[_row_spec(tm, LAT_W), whole(w_q), whole(w_kv), whole(qa_gain), whole(kva_gain), slab, slab, vec, vec],
                  out_specs=[_row_spec(tm, QL), _row_spec(tm, KVL), _row_spec(tm, NH * LANE), _row_spec(tm, NH * LANE + NH * NOPE),
                             _row_spec(tm, NH * LANE), _row_spec(tm, NH * LANE), _row_spec(tm, NH * NOPE)],
                  out_shape=[_sds((T, QL), MX), _sds((T, KVL), MX), _sds((T, NH * LANE), F32), _sds((T, NH * LANE + NH * NOPE), F32),
                             _sds((T, NH * LANE), MX), _sds((T, NH * LANE), MX), _sds((T, NH * NOPE), MX)],
                  )(z, w_q, w_kv, qa_gain, kva_gain, cos, sin, q_gain, k_gain)


def _mla_in_bwd(dq, dk, dv, qp, kvp, z, w_q, w_kv, qa_gain, kva_gain, cos, sin, q_gain, k_gain, *, nb, name, tm=256):
    T = dq.shape[0]
    tps = (T // nb) // tm
    slab = pl.BlockSpec((tm, LANE), lambda i: (i, 0))
    vec = pl.BlockSpec((1, LANE), lambda i: (0, 0))
    whole = lambda a: pl.BlockSpec(a.shape, lambda i: (0, 0))

    def latent_bwd(x, dy, gain):
        r = _rsq(x)
        xhat = x * r
        dxhat = dy * gain
        return r * (dxhat - xhat * jnp.mean(dxhat * xhat, axis=-1, keepdims=True)), jnp.sum(dy * xhat, axis=0, keepdims=True)

    def body(dq_ref, dk_ref, dv_ref, qp_ref, kvp_ref, z_ref, wq_ref, wkv_ref, qa_ref, kva_ref, cos_ref, sin_ref, qg_ref, kg_ref,
             dqp_ref, dkvp_ref, dkr_ref, dql_ref, dkvl_ref, st_ref, sq_ref, sk_ref):
        i = pl.program_id(0)
        qp, kvp = qp_ref, kvp_ref
        m_n, m_r, first_half = _lane_masks((tm, LANE))
        c = cos_ref[...]
        s = sin_ref[...]
        qg = qg_ref[...]
        kg = kg_ref[...]
        dqg = jnp.zeros((1, LANE), F32)
        dkg = jnp.zeros((1, LANE), F32)
        dkr_sum = jnp.zeros((tm, LANE), F32)
        for h in range(NH):
            x = qp[:, h * LANE:(h + 1) * LANE]
            x2 = x * x
            rn = lax.rsqrt(jnp.sum(jnp.where(m_n, x2, 0.0), axis=-1, keepdims=True) * (1.0 / NOPE) + EPS)
            rq = lax.rsqrt(jnp.sum(jnp.where(m_r, x2, 0.0), axis=-1, keepdims=True) * (1.0 / ROPE) + EPS)
            rfac = jnp.where(m_n, rn, jnp.where(m_r, rq, 0.0))
            xhat = x * rfac
            do = dq_ref[:, h * LANE:(h + 1) * LANE] * ATTN_SCALE
            dy = do * c + _rot_t(do * s, first_half, m_r)
            dqg = dqg + jnp.sum(dy * xhat, axis=0, keepdims=True)
            dxhat = dy * qg
            t = dxhat * xhat
            mean_n = jnp.sum(jnp.where(m_n, t, 0.0), axis=-1, keepdims=True) * (1.0 / NOPE)
            mean_r = jnp.sum(jnp.where(m_r, t, 0.0), axis=-1, keepdims=True) * (1.0 / ROPE)
            dqp_ref[:, h * LANE:(h + 1) * LANE] = (
                rfac * (dxhat - xhat * jnp.where(m_n, mean_n, jnp.where(m_r, mean_r, 0.0)))).astype(dqp_ref.dtype)

            xk = kvp[:, h * LANE:(h + 1) * LANE]
            rk = lax.rsqrt(jnp.sum(jnp.where(m_n, xk * xk, 0.0), axis=-1, keepdims=True) * (1.0 / NOPE) + EPS)
            khat = jnp.where(m_n, xk * rk, 0.0)
            dko = dk_ref[:, h * LANE:(h + 1) * LANE]
            dkn = jnp.where(m_n, dko, 0.0)
            dkg = dkg + jnp.sum(dkn * khat, axis=0, keepdims=True)
            dkhat = dkn * kg
            mean_k = jnp.sum(dkhat * khat, axis=-1, keepdims=True) * (1.0 / NOPE)
            dkvp_ref[:, h * LANE:(h + 1) * LANE] = jnp.where(m_n, rk * (dkhat - khat * mean_k), 0.0).astype(dkvp_ref.dtype)
            dkr_sum = dkr_sum + jnp.where(m_r, dko, 0.0)
        dkvp_ref[:, NH * LANE:] = dv_ref[...].astype(dkvp_ref.dtype)

        xr = z_ref[:, Z_KR:Z_KR + LANE].astype(F32)
        rr = lax.rsqrt(jnp.sum(xr * xr, axis=-1, keepdims=True) * (1.0 / ROPE) + EPS)
        rhat = xr * rr
        dyr = dkr_sum * c + _rot_t(dkr_sum * s, first_half, m_r)
        dkg = dkg + jnp.sum(dyr * rhat, axis=0, keepdims=True)
        drhat = dyr * kg
        mean_kr = jnp.sum(drhat * rhat, axis=-1, keepdims=True) * (1.0 / ROPE)
        dkr_ref[...] = jnp.where(m_r, rr * (drhat - rhat * mean_kr), 0.0).astype(dkr_ref.dtype)

        nt = (((1,), (1,)), ((), ()))
        dqn = lax.dot_general(dqp_ref[...], wq_ref[...], nt, preferred_element_type=F32)
        dkvn = lax.dot_general(dkvp_ref[...], wkv_ref[...], nt, preferred_element_type=F32)
        dql, dqa = latent_bwd(z_ref[:, Z_QL:Z_QL + QL].astype(F32), dqn, qa_ref[...])
        dkvl, dkva = latent_bwd(z_ref[:, Z_KV:Z_KV + KVL].astype(F32), dkvn, kva_ref[...])
        dql_ref[...] = dql.astype(dql_ref.dtype)
        dkvl_ref[...] = dkvl.astype(dkvl_ref.dtype)

        @pl.when(i % tps == 0)
        def _():
            st_ref[...] = jnp.zeros_like(st_ref)
            sq_ref[...] = jnp.zeros_like(sq_ref)
            sk_ref[...] = jnp.zeros_like(sk_ref)

        st_ref[0, 0:1, :] += dqg
        st_ref[0, 1:2, :] += dkg
        sq_ref[0, 0:1, :] += dqa
        sk_ref[0, 0:1, :] += dkva

    stat = lambda w: pl.BlockSpec((1, SUBLANE, w), lambda i: (i // tps, 0, 0))
    return _pcall(body, name=name, grid=(T // tm,),
                  in_specs=[_row_spec(tm, NH * LANE), _row_spec(tm, NH * LANE), _row_spec(tm, NH * NOPE),
                            _row_spec(tm, NH * LANE), _row_spec(tm, NH * LANE + NH * NOPE), _row_spec(tm, LAT_W),
                            whole(w_q), whole(w_kv), whole(qa_gain), whole(kva_gain), slab, slab, vec, vec],
                  out_specs=[_row_spec(tm, NH * LANE), _row_spec(tm, NH * LANE + NH * NOPE), slab, _row_spec(tm, QL),
                             _row_spec(tm, KVL), stat(LANE), stat(QL), stat(KVL)],
                  out_shape=[_sds((T, NH * LANE), MX), _sds((T, NH * LANE + NH * NOPE), MX), _sds((T, LANE), MX),
                             _sds((T, QL), MX), _sds((T, KVL), MX), _sds((nb, SUBLANE, LANE), F32),
                             _sds((nb, SUBLANE, QL), F32), _sds((nb, SUBLANE, KVL), F32)],
                  )(dq, dk, dv, qp, kvp, z, w_q, w_kv, qa_gain, kva_gain, cos, sin, q_gain, k_gain)


def _lower_triangle(t):
    return lax.broadcasted_iota(jnp.int32, (t, t), 1) <= lax.broadcasted_iota(jnp.int32, (t, t), 0)


def _attn_fwd(q, k, v, *, nb, name, tq=512, comm=None):
    T = q.shape[0]
    S = T // nb
    tq = _pick(S, tq)
    nq = S // tq
    tk = tq
    npair = NH // 2

    def body(q_ref, k_ref, v_ref, o_ref, lse_ref):
        qi = pl.program_id(2)
        lane = lax.broadcasted_iota(jnp.int32, (tq, LANE), 1)
        qs = [q_ref[:, hh * LANE:(hh + 1) * LANE] for hh in range(2)]

        def block(j, carry, diagonal):
            k0 = pl.multiple_of(j * tk, tk)
            vb = v_ref[pl.ds(k0, tk), :]
            new = []
            for hh in range(2):
                m, l, acc = carry[hh]
                kb = k_ref[pl.ds(k0, tk), hh * LANE:(hh + 1) * LANE]
                s = lax.dot_general(qs[hh], kb, (((1,), (1,)), ((), ())), preferred_element_type=F32)
                if diagonal:
                    s = jnp.where(_lower_triangle(tq), s, NEG)
                m_new = jnp.maximum(m, jnp.max(s, axis=-1, keepdims=True))
                p = jnp.exp(s - m_new)
                alpha = jnp.exp(m - m_new)
                l = alpha * l + jnp.sum(p, axis=-1, keepdims=True)
                acc = alpha * acc + jnp.dot(p.astype(MX), vb, preferred_element_type=F32)
                new.append((m_new, l, acc))
            return tuple(new)

        init = tuple((jnp.full((tq, 1), NEG, F32), jnp.zeros((tq, 1), F32), jnp.zeros((tq, LANE), F32)) for _ in range(2))
        carry = lax.fori_loop(0, qi, lambda j, c: block(j, c, False), init)
        (m0, l0, acc0), (m1, l1, acc1) = block(qi, carry, True)
        o_ref[...] = jnp.where(lane < NOPE, acc0 / l0, acc1 / l1).astype(o_ref.dtype)
        lse_ref[...] = jnp.where(lane < NOPE, m0 + jnp.log(l0), m1 + jnp.log(l1))

    return _pcall(body, name=name, grid=(nb, npair, nq),
                  in_specs=[pl.BlockSpec((tq, 2 * LANE), lambda b, p, i: (b * nq + i, p)),
                            pl.BlockSpec((S, 2 * LANE), lambda b, p, i: (b, p)),
                            pl.BlockSpec((S, LANE), lambda b, p, i: (b, p))],
                  out_specs=[pl.BlockSpec((tq, LANE), lambda b, p, i: (b * nq + i, p)),
                             pl.BlockSpec((tq, LANE), lambda b, p, i: (b * nq + i, p))],
                  out_shape=[_sds((T, NH * NOPE), MX), _sds((T, NH * NOPE), F32)], comm=comm)(q, k, v)


def _attn_bwd(q, k, v, o, lse, do, *, nb, name, tq=512, comm=None):
    T = q.shape[0]
    S = T // nb
    tq = _pick(S, tq)
    nq = S // tq
    tk = tq
    npair = NH // 2

    def body(q_ref, k_ref, v_ref, o_ref, lse_ref, do_ref, dq_ref, dk_ref, dv_ref, delta_ref):
        lane = lax.broadcasted_iota(jnp.int32, (tq, LANE), 1)
        first = lane < NOPE
        dq_ref[...] = jnp.zeros_like(dq_ref)

        def delta_step(qi, _):
            q0 = pl.multiple_of(qi * tq, tq)
            prod = do_ref[pl.ds(q0, tq), :] * o_ref[pl.ds(q0, tq), :].astype(F32)
            d0 = jnp.sum(jnp.where(first, prod, 0.0), axis=-1, keepdims=True)
            d1 = jnp.sum(jnp.where(first, 0.0, prod), axis=-1, keepdims=True)
            delta_ref[pl.ds(q0, tq), :] = jnp.where(first, d0, d1)
            return 0

        lax.fori_loop(0, nq, delta_step, 0)

        def kv_step(kj, _):
            k0 = pl.multiple_of(kj * tk, tk)
            kbs = [k_ref[pl.ds(k0, tk), hh * LANE:(hh + 1) * LANE] for hh in range(2)]
            vb = v_ref[pl.ds(k0, tk), :]

            def q_block(qi, carry, diagonal):
                dk0, dk1, dv = carry
                dks = [dk0, dk1]
                q0 = pl.multiple_of(qi * tq, tq)
                dov = do_ref[pl.ds(q0, tq), :]
                lse_v = lse_ref[pl.ds(q0, tq), :]
                delta_v = delta_ref[pl.ds(q0, tq), :]
                for hh in range(2):
                    qh = q_ref[pl.ds(q0, tq), hh * LANE:(hh + 1) * LANE]
                    dob = jnp.where(first if hh == 0 else jnp.logical_not(first), dov, 0.0).astype(MX)
                    s = lax.dot_general(qh, kbs[hh], (((1,), (1,)), ((), ())), preferred_element_type=F32)
                    p = jnp.exp(s - lse_v[:, hh * NOPE:hh * NOPE + 1])
                    if diagonal:
                        p = jnp.where(_lower_triangle(tq), p, 0.0)
                    dp = lax.dot_general(dob, vb, (((1,), (1,)), ((), ())), preferred_element_type=F32)
                    ds = (p * (dp - delta_v[:, hh * NOPE:hh * NOPE + 1])).astype(MX)
                    dks[hh] = dks[hh] + lax.dot_general(ds, qh, (((0,), (0,)), ((), ())), preferred_element_type=F32)
                    dv = dv + lax.dot_general(p.astype(MX), dob, (((0,), (0,)), ((), ())), preferred_element_type=F32)
                    dq_ref[pl.ds(q0, tq), hh * LANE:(hh + 1) * LANE] += jnp.dot(ds, kbs[hh], preferred_element_type=F32)
                return dks[0], dks[1], dv

            zero = jnp.zeros((tk, LANE), F32)
            carry = q_block(kj, (zero, zero, zero), True)
            dk0, dk1, dv = lax.fori_loop(kj + 1, nq, lambda qi, c: q_block(qi, c, False), carry)
            dk_ref[pl.ds(k0, tk), 0:LANE] = dk0
            dk_ref[pl.ds(k0, tk), LANE:2 * LANE] = dk1
            dv_ref[pl.ds(k0, tk), :] = dv
            return 0

        lax.fori_loop(0, nq, kv_step, 0)

    pair256 = pl.BlockSpec((S, 2 * LANE), lambda b, p: (b, p))
    pair128 = pl.BlockSpec((S, LANE), lambda b, p: (b, p))
    return _pcall(body, name=name, grid=(nb, npair),
                  in_specs=[pair256, pair256, pair128, pair128, pair128, pair128],
                  out_specs=[pair256, pair256, pair128],
                  out_shape=[_sds((T, NH * LANE), F32), _sds((T, NH * LANE), F32), _sds((T, NH * NOPE), F32)],
                  scratch=[pltpu.VMEM((S, LANE), F32)], comm=comm)(q, k, v, o, lse, do)


def _run(plan, fn, *args, name, **kw):
    rider = plan.rider(name)
    if rider is None:
        return fn(*args, name=name, **kw)
    outs, landed = fn(*args, name=name, comm=rider, **kw)
    plan.landed(name, landed)
    return outs


def _layer_fwd_bwd(x, tgt, mod3, cos, sin, plan, P):
    nb = mod3.shape[0]
    W = plan.W
    h1, gu1, a1 = _run(plan, _ffn_up_first, x, mod3, P["norm_ffn1"], W["ffn1_in"], sub=0, name="ffn1_up_a")
    gu1, a1 = _run(plan, _ffn_up_second, h1, W["ffn1_in"], gu1, a1, name="ffn1_up_b")
    f1, x1, h2 = _run(plan, _mm_resid_norm, a1, W["ffn1_out"], x, mod3, P["norm_mix"], sub=1, coef=0.5, name="ffn1_out")
    z = _run(plan, _mm, h2, W["w_in"], mode="nt", out_dtype=MX, name="mix_in", tn=1664)
    pooled, mixed, scaled = _pool_fwd(z, W["pool_grp"], P["pool_scale"], nb=nb, name="pool_fwd")
    br_pool = _mm(scaled, W["pool_proj"], mode="nn", out_dtype=MX, name="pool_proj")
    qn, kvn, qp, kvp, q, k, v = _mla_in_fwd(z, W["q_up"], W["kv_up"], P["q_a_norm"], P["kv_a_norm"], cos, sin,
                                            P["q_gain"], P["k_gain"], name="mla_in_fwd")
    attn, lse = _run(plan, _attn_fwd, q, k, v, nb=nb, name="attn_fwd")
    br_mla, merged, mo, x2, h3 = _mix_out_fwd(z, br_pool, attn, W["mla_proj"], W["w_out"], x1, mod3, P["norm_ffn2"], sub=2,
                                              name="mix_out", tm=256)
    gu2, a2 = _ffn_up(h3, W["ffn2_in"], name="ffn2_up")
    dy, df2, st_fin, loss = _mm_loss(a2, W["ffn2_out"], x2, tgt, mod3, name="ffn2_out_loss")

    plan.grad("ffn2_out", _dw(a2, df2, name="d_ffn2_out"))
    dgu2 = _ffn_dact(df2, gu2, W["ffn2_out"], name="ffn2_dact")
    plan.grad("ffn2_in", _ffn_dw_in(h3, dgu2, name="d_ffn2_in"))
    dx2, dmo, st3 = _run(plan, _ffn_dh_norm, dgu2, W["ffn2_in"], x2, dy, mo, mod3, P["norm_ffn2"], sub=2, coef=1.0,
                         name="d_ffn2_h")
    plan.grad("w_out", _dw(merged, dmo, name="d_mix_out"))
    dbr_pool, dbr_mla, dgates, dattn = _mix_out_bwd(dmo, W["w_out"], z, br_pool, br_mla, W["mla_proj"], name="mix_out_bwd", tm=256)
    plan.grad("pool_proj", _dw(scaled, dbr_pool, name="d_pool_proj"))
    dscaled = _mm(dbr_pool, W["pool_proj"], mode="nt", out_dtype=F32, name="d_pool_scaled")
    du_pool, d_pool_grp, d_pool_scale = _pool_bwd(dscaled, mixed, pooled, W["pool_grp"], P["pool_scale"], nb=nb, name="pool_bwd")
    plan.grad("mla_proj", _dw(attn, dbr_mla, name="d_mla_proj"))
    dq, dk, dv = _run(plan, _attn_bwd, q, k, v, attn, lse, dattn, nb=nb, name="attn_bwd")
    dqp, dkvp, dkr, dql, dkvl, st_prep, st_q, st_kv = _mla_in_bwd(
        dq, dk, dv, qp, kvp, z, W["q_up"], W["kv_up"], P["q_a_norm"], P["kv_a_norm"], cos, sin, P["q_gain"], P["k_gain"],
        nb=nb, name="mla_in_bwd")
    plan.grad("q_up", _dw(qn, dqp, name="d_q_up"))
    plan.grad("kv_up", _dw(kvn, dkvp, name="d_kv_up"))
    dz = jnp.concatenate([du_pool, dql, dkvl, dkr, dgates], axis=1)
    plan.grad("w_in", _run(plan, _dw, h2, dz, name="d_mix_in", tm=256, tn=1664, out_t=True))
    dx1, df1, st2 = _run(plan, _mm_norm_bwd, dz, W["w_in"], x1, dx2, f1, mod3, P["norm_mix"], sub=1, coef=0.5, name="d_mix_h")
    plan.grad("ffn1_out", _run(plan, _dw, a1, df1, name="d_ffn1_out"))
    dgu1 = _run(plan, _ffn_dact, df1, gu1, W["ffn1_out"], name="ffn1_dact")
    half = x.shape[1] // 2
    plan.grad("ffn1_in@0", _run(plan, _ffn_dw_in, h1, dgu1, rows=(0, half), name="d_ffn1_in_a"))
    plan.grad("ffn1_in@1", _run(plan, _ffn_dw_in, h1, dgu1, rows=(half, half), name="d_ffn1_in_b"))
    dh1 = _run(plan, _ffn_dh, dgu1, W["ffn1_in"], name="d_ffn1_h")
    grad_x, st1 = _run(plan, _bwd_block, x, dh1, dx1, None, mod3, P["norm_ffn1"], sub=0, coef=0.0, name="bwd_norm1")

    return loss, grad_x, (st1, st2, st3, st_fin, st_q, st_kv, st_prep, d_pool_scale), d_pool_grp


def _padded(rows):
    return rows + -rows % (4 * SUBLANE)


def _pad_rows(a):
    pad = [(0, 0)] * a.ndim
    pad[-2] = (0, _padded(a.shape[-2]) - a.shape[-2])
    return jnp.pad(a, pad)


def _w_in_pieces(shard):
    runs = []
    for ref0, ref1, k0 in ((0, 1152, 0), (1152, 1184, Z_KR + KR_LANE), (1184, N_CHIP * shard, Z_GP)):
        for q in range(N_CHIP):
            lo, hi = max(ref0, q * shard), min(ref1, (q + 1) * shard)
            if lo < hi:
                runs.append((k0 + lo - ref0, q * _padded(shard) + lo - q * shard, hi - lo))
    return runs


def _w_in_stacked_to_kernel(st, shard):
    flat = st.reshape(-1, st.shape[2])
    parts, at = [], 0
    for k0, s0, n in sorted(_w_in_pieces(shard)):
        parts += [jnp.zeros((k0 - at, flat.shape[1]), st.dtype)] * (k0 > at) + [flat[s0:s0 + n]]
        at = k0 + n
    assert at == Z_W
    return jnp.concatenate(parts, axis=0)


def _w_in_kernel_to_stacked(gt, shard):
    parts, at = [], 0
    for s0, k0, n in sorted((s, k, n) for k, s, n in _w_in_pieces(shard)) + [(N_CHIP * _padded(shard), 0, 0)]:
        parts += [jnp.zeros((s0 - at, gt.shape[1]), gt.dtype)] * (s0 > at) + [gt[k0:k0 + n]] * (n > 0)
        at = s0 + n
    return jnp.concatenate(parts, axis=0).reshape(N_CHIP, _padded(shard), gt.shape[1])


def _q_up_to_kernel(w):
    k = w.shape[0]
    return jnp.pad(w.reshape(k, NH, NOPE + ROPE), ((0, 0), (0, 0), (0, LANE - NOPE - ROPE))).reshape(k, NH * LANE)


def _q_up_from_kernel(g):
    k = g.shape[0]
    return g.reshape(k, NH, LANE)[:, :, :NOPE + ROPE].reshape(k, NH * (NOPE + ROPE))


def _kv_up_to_kernel(w):
    k = w.shape[0]
    w3 = w.reshape(k, NH, 2 * NOPE)
    kpart = jnp.pad(w3[:, :, :NOPE], ((0, 0), (0, 0), (0, LANE - NOPE))).reshape(k, NH * LANE)
    return jnp.concatenate([kpart, w3[:, :, NOPE:].reshape(k, NH * NOPE)], axis=1)


def _kv_up_from_kernel(g):
    k = g.shape[0]
    kpart = g[:, :NH * LANE].reshape(k, NH, LANE)[:, :, :NOPE]
    vpart = g[:, NH * LANE:].reshape(k, NH, NOPE)
    return jnp.concatenate([kpart, vpart], axis=2).reshape(k, NH * 2 * NOPE)


def _gain_slab(nope, rope):
    return jnp.concatenate([nope, rope, jnp.zeros((1, LANE - NOPE - ROPE), nope.dtype)], axis=1)


def _rope_tables(positions):
    inv_freq = 10000.0 ** (-jnp.arange(0, ROPE, 2, dtype=F32) / ROPE)
    ang = positions.astype(F32)[:, None] * inv_freq
    ang = jnp.concatenate([ang, ang], axis=-1)
    t = positions.shape[0]
    cos = jnp.concatenate([jnp.ones((t, KR_LANE), F32), jnp.cos(ang), jnp.ones((t, LANE - KR_LANE - ROPE), F32)], axis=1)
    sin = jnp.concatenate([jnp.zeros((t, KR_LANE), F32), jnp.sin(ang), jnp.zeros((t, LANE - KR_LANE - ROPE), F32)], axis=1)
    return cos, sin


def _coords():
    return lax.axis_index("x"), lax.axis_index("y"), lax.axis_index("c")


HBM_SPEC = pl.BlockSpec(memory_space=pl.ANY)
VMEM_SPEC = pl.BlockSpec(memory_space=pltpu.VMEM)


CHIP_RELS = ((1, 0), (0, 1), (1, 1))


def _gather_comm(shards):
    n = len(shards)

    def ici(ins, outs, sems, a, j, x, y, cc):
        half = ins[a].shape[0] // 2
        mine = pl.ds(cc * half, half)
        dx, dy = CHIP_RELS[j]
        return pltpu.make_async_remote_copy(src_ref=ins[a].at[mine], dst_ref=outs[a].at[2 * x + y, mine],
                                            send_sem=sems[0].at[a, j], recv_sem=sems[1].at[a, j],
                                            device_id=(x ^ dx, y ^ dy, cc), device_id_type=MESH)

    def d2d(ins, outs, sems, a, j, x, y, cc, half_of):
        half = ins[a].shape[0] // 2
        dx, dy = CHIP_RELS[j]
        landed = outs[a].at[2 * (x ^ dx) + (y ^ dy), pl.ds(half_of * half, half)]
        return pltpu.make_async_remote_copy(src_ref=landed, dst_ref=landed, send_sem=sems[2].at[a, j], recv_sem=sems[3].at[a, j],
                                            device_id=(x, y, 1 - cc), device_id_type=MESH)

    def own(ins, outs, sems, a, x, y):
        return pltpu.make_async_copy(ins[a], outs[a].at[2 * x + y], sems[4].at[a])

    def start(ins, outs, sems):
        x, y, cc = _coords()
        for a in range(n):
            own(ins, outs, sems, a, x, y).start()
            for j in range(3):
                ici(ins, outs, sems, a, j, x, y, cc).start()

    def finish(ins, outs, sems):
        x, y, cc = _coords()
        for a in range(n):
            for j in range(3):
                ici(ins, outs, sems, a, j, x, y, cc).wait_recv()
                d2d(ins, outs, sems, a, j, x, y, cc, cc).start()
        for a in range(n):
            for j in range(3):
                d2d(ins, outs, sems, a, j, x, y, cc, 1 - cc).wait_recv()
        for a in range(n):
            for j in range(3):
                ici(ins, outs, sems, a, j, x, y, cc).wait_send()
                d2d(ins, outs, sems, a, j, x, y, cc, cc).wait_send()
            own(ins, outs, sems, a, x, y).wait()

    dma = pltpu.SemaphoreType.DMA
    return _Comm(shards, [_sds((N_CHIP,) + s.shape, s.dtype) for s in shards],
                 [dma((n, 3)), dma((n, 3)), dma((n, 3)), dma((n, 3)), dma((n,))], start, finish)


def _swap_comm(parts):
    n = len(parts)

    def copy(ins, outs, sems, a):
        x, y, cc = _coords()
        half = ins[a].shape[1] // 2
        return pltpu.make_async_remote_copy(src_ref=ins[a].at[:, pl.ds((1 - cc) * half, half)], dst_ref=outs[a],
                                            send_sem=sems[0].at[a], recv_sem=sems[1].at[a], device_id=(x, y, 1 - cc),
                                            device_id_type=MESH)

    def start(ins, outs, sems):
        for a in range(n):
            copy(ins, outs, sems, a).start()

    def finish(ins, outs, sems):
        for a in range(n):
            copy(ins, outs, sems, a).wait()

    dma = pltpu.SemaphoreType.DMA
    return _Comm(parts, [_sds((p.shape[0], p.shape[1] // 2, p.shape[2]), p.dtype) for p in parts], [dma((n,)), dma((n,))],
                 start, finish)


def _add_half(full, other, cidx, *, name):
    nch, r, c = full.shape
    half = r // 2
    tr = _pick_rows(half)
    nbk = half // tr
    grid_spec = pltpu.PrefetchScalarGridSpec(
        num_scalar_prefetch=1, grid=(nch, nbk),
        in_specs=[pl.BlockSpec((1, tr, c), lambda j, i, cref: (j, cref[0] * nbk + i, 0)),
                  pl.BlockSpec((1, tr, c), lambda j, i, cref: (j, i, 0))],
        out_specs=pl.BlockSpec((1, tr, c), lambda j, i, cref: (j, i, 0)))

    def body(cref, a_ref, b_ref, o_ref):
        o_ref[...] = (a_ref[...] + b_ref[...]).astype(o_ref.dtype)

    return _pcall(body, name=name, out_shape=_sds((nch, half, c), MX), grid_spec=grid_spec)(cidx, full, other)


def _pick_rows(rows, target=512):
    best = None
    for t in range(16, min(rows, target) + 1, 16):
        if rows % t == 0:
            best = t
    return rows if best is None else best


def _exchange_comm(parts):
    n = len(parts)

    def send(ins, outs, sems, a, j, x, y, cc):
        dx, dy = CHIP_RELS[j]
        return pltpu.make_async_remote_copy(src_ref=ins[a].at[2 * (x ^ dx) + (y ^ dy)], dst_ref=outs[a].at[2 * x + y],
                                            send_sem=sems[0].at[a, j], recv_sem=sems[1].at[a, j],
                                            device_id=(x ^ dx, y ^ dy, cc), device_id_type=MESH)

    def landing(ins, outs, sems, a, j, x, y, cc):
        dx, dy = CHIP_RELS[j]
        peer_chip = 2 * (x ^ dx) + (y ^ dy)
        return pltpu.make_async_remote_copy(src_ref=ins[a].at[peer_chip], dst_ref=outs[a].at[peer_chip], send_sem=sems[0].at[a, j],
                                            recv_sem=sems[1].at[a, j], device_id=(x, y, cc), device_id_type=MESH)

    def own(ins, outs, sems, a, x, y):
        return pltpu.make_async_copy(ins[a].at[2 * x + y], outs[a].at[2 * x + y], sems[2].at[a])

    def start(ins, outs, sems):
        x, y, cc = _coords()
        for a in range(n):
            own(ins, outs, sems, a, x, y).start()
            for j in range(3):
                send(ins, outs, sems, a, j, x, y, cc).start()

    def finish(ins, outs, sems):
        x, y, cc = _coords()
        for a in range(n):
            for j in range(3):
                landing(ins, outs, sems, a, j, x, y, cc).wait_recv()
        for a in range(n):
            for j in range(3):
                send(ins, outs, sems, a, j, x, y, cc).wait_send()
            own(ins, outs, sems, a, x, y).wait()

    dma = pltpu.SemaphoreType.DMA
    return _Comm(parts, [_sds(p.shape, p.dtype) for p in parts], [dma((n, 3)), dma((n, 3)), dma((n,))], start, finish)


def _sum_chips(q, cidx, *, name):
    nch, h, c = q.shape
    tr = _pick_rows(h)
    nbk = h // tr
    grid_spec = pltpu.PrefetchScalarGridSpec(
        num_scalar_prefetch=1, grid=(nbk,),
        in_specs=[pl.BlockSpec((nch, tr, c), lambda i, cref: (0, i, 0))],
        out_specs=pl.BlockSpec((tr, c), lambda i, cref: (cref[0] * nbk + i, 0)))

    def body(cref, q_ref, o_ref):
        acc = q_ref[0].astype(F32) + q_ref[1].astype(F32)
        acc = acc + q_ref[2].astype(F32)
        o_ref[...] = acc + q_ref[3].astype(F32)

    return _pcall(body, name=name, out_shape=_sds((2 * h, c), F32), grid_spec=grid_spec)(cidx, q)


def _join_comm(fulls):
    n = len(fulls)

    def half_copy(outs, sems, a, which):
        x, y, cc = _coords()
        h = outs[a].shape[0] // 2
        rows = outs[a].at[pl.ds((cc if which == 0 else 1 - cc) * h, h)]
        return pltpu.make_async_remote_copy(src_ref=rows, dst_ref=rows, send_sem=sems[0].at[a], recv_sem=sems[1].at[a],
                                            device_id=(x, y, 1 - cc), device_id_type=MESH)

    def start(ins, outs, sems):
        for a in range(n):
            half_copy(outs, sems, a, 0).start()

    def finish(ins, outs, sems):
        for a in range(n):
            half_copy(outs, sems, a, 1).wait_recv()
        for a in range(n):
            half_copy(outs, sems, a, 0).wait_send()

    dma = pltpu.SemaphoreType.DMA
    return _Comm(fulls, [_sds(p.shape, p.dtype) for p in fulls], [dma((n,)), dma((n,))], start, finish,
                 aliases={a: a for a in range(n)})


def _ada_prologue(c, w, b, *, name, comm=None):
    nb, dm = c.shape
    n = w.shape[1]

    def body(c_ref, w_ref, b_ref, call_ref, land_ref, cpad, mod_s, g_send, g_recv, m_send, m_recv):
        x, y, cc = _coords()
        me = 4 * x + 2 * y + cc
        chip = 2 * x + y
        cpad[...] = jnp.zeros_like(cpad)
        cpad[0:nb, :] = c_ref[...]
        copies = []
        for kk in range(1, N_DEV):
            peer = (x ^ (kk >> 2), y ^ ((kk >> 1) & 1), cc ^ (kk & 1))
            cp = pltpu.make_async_remote_copy(src_ref=cpad, dst_ref=call_ref.at[me], send_sem=g_send.at[kk - 1],
                                              recv_sem=g_recv.at[kk - 1], device_id=peer, device_id_type=MESH)
            cp.start()
            copies.append(cp)
        call_ref[me] = cpad[...]
        for kk in range(1, N_DEV):
            pltpu.make_async_remote_copy(src_ref=cpad, dst_ref=call_ref.at[me ^ kk], send_sem=g_send.at[kk - 1],
                                         recv_sem=g_recv.at[kk - 1], device_id=(x, y, cc), device_id_type=MESH).wait_recv()
        cv = call_ref[...].reshape(N_DEV * SUBLANE, dm)
        act = (cv * _sigmoid(cv)).astype(MX)
        mod = jnp.dot(act, w_ref[...].astype(MX), preferred_element_type=F32) + b_ref[...]
        mod_s[...] = mod.reshape(N_DEV, SUBLANE, n)
        for j, (dx, dy) in enumerate(CHIP_RELS):
            cp = pltpu.make_async_remote_copy(src_ref=mod_s.at[4 * (x ^ dx) + 2 * (y ^ dy) + cc], dst_ref=land_ref.at[chip],
                                              send_sem=m_send.at[j], recv_sem=m_recv.at[j],
                                              device_id=(x ^ dx, y ^ dy, cc), device_id_type=MESH)
            cp.start()
            copies.append(cp)
        land_ref[chip] = mod_s[me]
        for j, (dx, dy) in enumerate(CHIP_RELS):
            pltpu.make_async_remote_copy(src_ref=mod_s.at[me], dst_ref=land_ref.at[2 * (x ^ dx) + (y ^ dy)], send_sem=m_send.at[j],
                                         recv_sem=m_recv.at[j], device_id=(x, y, cc), device_id_type=MESH).wait_recv()
        for cp in copies:
            cp.wait_send()

    dma = pltpu.SemaphoreType.DMA
    return _pcall(body, name=name, in_specs=[VMEM_SPEC] * 3, out_specs=[VMEM_SPEC] * 2,
                  out_shape=[_sds((N_DEV, SUBLANE, dm), F32), _sds((N_CHIP, SUBLANE, n), F32)],
                  scratch=[pltpu.VMEM((SUBLANE, dm), F32), pltpu.VMEM((N_DEV, SUBLANE, n), F32),
                           dma((N_DEV - 1,)), dma((N_DEV - 1,)), dma((3,)), dma((3,))], comm=comm)(c, w, b)


def _ada_bwd(c_all, dmod_cols, *, name):
    m, kdim = c_all.shape
    n = dmod_cols.shape[1]
    tn = _pick(n, 1152)

    def body(c_ref, d_ref, o_ref):
        cv = c_ref[...]
        act = (cv * _sigmoid(cv)).astype(MX)
        o_ref[...] = lax.dot_general(act, d_ref[...].astype(MX), (((0,), (0,)), ((), ())), preferred_element_type=F32)

    return _pcall(body, name=name, out_shape=_sds((kdim, n), F32), grid=(n // tn,),
                  in_specs=[pl.BlockSpec((m, kdim), lambda j: (0, 0)), pl.BlockSpec((m, tn), lambda j: (0, j))],
                  out_specs=pl.BlockSpec((kdim, tn), lambda j: (0, j)))(c_all, dmod_cols)


SLAB_W = 1024
RED_ROWS = 8


LOSS_LANE = SLAB_W - LANE


def _pack_stats(st1, st2, st3, st_fin, st_q, st_kv, st_prep, d_pool_scale, loss8, *, name):
    nb = st1.shape[0]
    dm_rows = -(-9 * nb // SUBLANE) * SUBLANE

    def body(s1, s2, s3, sf, sq, skv, sp, sps, loss_ref, o_ref):
        o_ref[...] = jnp.zeros_like(o_ref)
        for s in range(nb):
            rows = [s1[s, 0:1, :], s1[s, 1:2, :], s2[s, 3:4, :], s2[s, 0:1, :], s2[s, 1:2, :], s3[s, 3:4, :], s3[s, 0:1, :],
                    s3[s, 1:2, :], sf[s, 0:1, :]]
            for k, row in enumerate(rows):
                o_ref[9 * s + k:9 * s + k + 1, :] = row

        def over_seq(ref, r):
            acc = ref[0, r:r + 1, :]
            for s in range(1, nb):
                acc = acc + ref[s, r:r + 1, :]
            return acc

        o_ref[dm_rows + 0:dm_rows + 1, :] = over_seq(s1, 2)
        o_ref[dm_rows + 1:dm_rows + 2, :] = over_seq(s2, 2)
        o_ref[dm_rows + 2:dm_rows + 3, :] = over_seq(s3, 2)
        o_ref[dm_rows + 3:dm_rows + 4, 0:POOL_W] = over_seq(sps, 0)
        o_ref[dm_rows + 4:dm_rows + 5, 0:QL] = over_seq(sq, 0)
        o_ref[dm_rows + 5:dm_rows + 6, 0:KVL] = over_seq(skv, 0)
        o_ref[dm_rows + 6:dm_rows + 7, 0:LANE] = over_seq(sp, 0)
        o_ref[dm_rows + 7:dm_rows + 8, 0:LANE] = over_seq(sp, 1)
        o_ref[dm_rows + 7:dm_rows + 8, LOSS_LANE:] = loss_ref[0:1, :]

    return _pcall(body, name=name, out_shape=_sds((dm_rows + RED_ROWS, SLAB_W), F32), in_specs=[VMEM_SPEC] * 9,
                  out_specs=VMEM_SPEC)(st1, st2, st3, st_fin, st_q, st_kv, st_prep, d_pool_scale, loss8)


def _small_allreduce(slab, pool, *, name, comm=None):
    rows, w = slab.shape
    dm = rows - RED_ROWS
    prow, pw = pool.shape
    crow = RED_ROWS + 2 * dm

    def body(slab_ref, pool_ref, red_ref, dmod_ref, ptot_ref, sib_slab, sib_pool, chip_slab, chip_pool, land_slab, land_pool,
             a_send, a_recv, b_send, b_recv):
        x, y, cc = _coords()
        chip = 2 * x + y
        sib = (x, y, 1 - cc)
        to_sib = [pltpu.make_async_remote_copy(src_ref=slab_ref, dst_ref=sib_slab, send_sem=a_send.at[0], recv_sem=a_recv.at[0],
                                               device_id=sib, device_id_type=MESH),
                  pltpu.make_async_remote_copy(src_ref=pool_ref, dst_ref=sib_pool, send_sem=a_send.at[1], recv_sem=a_recv.at[1],
                                               device_id=sib, device_id_type=MESH)]
        for cp in to_sib:
            cp.start()
        for cp in to_sib:
            cp.wait()
        mine_dm, theirs_dm = slab_ref[0:dm, :], sib_slab[0:dm, :]
        chip_slab[0:RED_ROWS, :] = slab_ref[dm:, :] + sib_slab[dm:, :]
        chip_slab[RED_ROWS:RED_ROWS + dm, :] = jnp.where(cc == 0, mine_dm, theirs_dm)
        chip_slab[RED_ROWS + dm:, :] = jnp.where(cc == 0, theirs_dm, mine_dm)
        chip_pool[...] = pool_ref[...] + sib_pool[...]

        sends = []
        for j, (dx, dy) in enumerate(CHIP_RELS):
            peer = (x ^ dx, y ^ dy, cc)
            sends.append(pltpu.make_async_remote_copy(src_ref=chip_slab, dst_ref=land_slab.at[chip], send_sem=b_send.at[j, 0],
                                                      recv_sem=b_recv.at[j, 0], device_id=peer, device_id_type=MESH))
            sends.append(pltpu.make_async_remote_copy(src_ref=chip_pool, dst_ref=land_pool.at[chip], send_sem=b_send.at[j, 1],
                                                      recv_sem=b_recv.at[j, 1], device_id=peer, device_id_type=MESH))
        for cp in sends:
            cp.start()
        land_slab[chip] = chip_slab[...]
        land_pool[chip] = chip_pool[...]
        for j, (dx, dy) in enumerate(CHIP_RELS):
            peer_chip = 2 * (x ^ dx) + (y ^ dy)
            pltpu.make_async_remote_copy(src_ref=chip_slab, dst_ref=land_slab.at[peer_chip], send_sem=b_send.at[j, 0],
                                         recv_sem=b_recv.at[j, 0], device_id=(x, y, cc), device_id_type=MESH).wait_recv()
            pltpu.make_async_remote_copy(src_ref=chip_pool, dst_ref=land_pool.at[peer_chip], send_sem=b_send.at[j, 1],
                                         recv_sem=b_recv.at[j, 1], device_id=(x, y, cc), device_id_type=MESH).wait_recv()
        red = land_slab[0, 0:RED_ROWS, :]
        ptot = land_pool[0]
        for ch in range(1, N_CHIP):
            red = red + land_slab[ch, 0:RED_ROWS, :]
            ptot = ptot + land_pool[ch]
        red_ref[...] = red
        ptot_ref[...] = ptot
        for ch in range(N_CHIP):
            dmod_ref[2 * dm * ch:2 * dm * (ch + 1), :] = land_slab[ch, RED_ROWS:, :]
        for cp in sends:
            cp.wait_send()

    dma = pltpu.SemaphoreType.DMA
    return _pcall(body, name=name, in_specs=[VMEM_SPEC, VMEM_SPEC], out_specs=[VMEM_SPEC] * 3,
                  out_shape=[_sds((RED_ROWS, w), F32), _sds((N_DEV * dm, w), F32), _sds((prow, pw), F32)],
                  scratch=[pltpu.VMEM((rows, w), F32), pltpu.VMEM((prow, pw), F32), pltpu.VMEM((crow, w), F32),
                           pltpu.VMEM((prow, pw), F32), pltpu.VMEM((N_CHIP, crow, w), F32), pltpu.VMEM((N_CHIP, prow, pw), F32),
                           dma((2,)), dma((2,)), dma((3, 2)), dma((3, 2))], comm=comm)(slab, pool)


def _adamw_math(w, g, m, v):
    mn = ADAM_B1 * m + (1.0 - ADAM_B1) * g
    vn = ADAM_B2 * v + (1.0 - ADAM_B2) * (g * g)
    bc1 = 1.0 / (1.0 - ADAM_B1 ** ADAM_STEP)
    bc2 = 1.0 / (1.0 - ADAM_B2 ** ADAM_STEP)
    return -ADAM_LR * ((mn * bc1) / (jnp.sqrt(vn * bc2) + ADAM_EPS) + ADAM_WD * w), mn, vn


def _small_update(red, dmod_all, pool_total, nb, params, *, name):
    names = list(SMALL)
    dm = dmod_all.shape[0] // N_DEV

    def grad_of(nm, red_ref, dmod_ref, ptot_ref):
        if nm == "b_ada":
            acc = None
            for d in range(N_DEV):
                for s in range(nb):
                    blk = dmod_ref[d * dm + 9 * s:d * dm + 9 * s + 9, :]
                    acc = blk if acc is None else acc + blk
            return jnp.concatenate([acc[k:k + 1, :] for k in range(9)], axis=1)
        if nm == "pool_grp":
            return ptot_ref[...]
        row, lo, n = {"norm_ffn1": (0, 0, D), "norm_mix": (1, 0, D), "norm_ffn2": (2, 0, D), "pool_scale": (3, 0, POOL_W),
                      "q_a_norm": (4, 0, QL), "kv_a_norm": (5, 0, KVL), "q_norm_nope": (6, 0, NOPE),
                      "q_norm_rope": (6, NOPE, ROPE), "k_norm_nope": (7, 0, NOPE), "k_norm_rope": (7, KR_LANE, ROPE)}[nm]
        return red_ref[row:row + 1, lo:lo + n]

    def body(*refs):
        red_ref, dmod_ref, ptot_ref = refs[:3]
        ins = refs[3:3 + 3 * len(names)]
        outs = refs[3 + 3 * len(names):]
        outs[4 * len(names)][...] = red_ref[RED_ROWS - 1:RED_ROWS, LOSS_LANE:]
        for i, nm in enumerate(names):
            g = grad_of(nm, red_ref, dmod_ref, ptot_ref)
            d, mn, vn = _adamw_math(ins[3 * i][...], g, ins[3 * i + 1][...], ins[3 * i + 2][...])
            outs[4 * i][...] = g
            outs[4 * i + 1][...] = d
            outs[4 * i + 2][...] = mn
            outs[4 * i + 3][...] = vn

    flat_in = [a for nm in names for a in params[nm]]
    out_shape = [_sds(params[nm][0].shape, F32) for nm in names for _ in range(4)] + [_sds((1, LANE), F32)]
    res = _pcall(body, name=name, in_specs=[VMEM_SPEC] * (3 + len(flat_in)), out_specs=[VMEM_SPEC] * len(out_shape),
                 out_shape=out_shape)(red, dmod_all, pool_total, *flat_in)
    return {nm: tuple(res[4 * i:4 * i + 4]) for i, nm in enumerate(names)}, res[-1]


def _adamw(w, g, m, v, *, name, comm=None):
    r, c = w.shape
    tr = _pick_rows(r, 256)
    tc = c if tr < r else _pick(c, 256)
    spec = pl.BlockSpec((tr, tc), lambda i, j: (i, j))
    bc1 = 1.0 / (1.0 - ADAM_B1 ** ADAM_STEP)
    bc2 = 1.0 / (1.0 - ADAM_B2 ** ADAM_STEP)

    def body(w_ref, g_ref, m_ref, v_ref, d_ref, mo_ref, vo_ref):
        gv = g_ref[...]
        mn = ADAM_B1 * m_ref[...] + (1.0 - ADAM_B1) * gv
        vn = ADAM_B2 * v_ref[...] + (1.0 - ADAM_B2) * (gv * gv)
        mo_ref[...] = mn
        vo_ref[...] = vn
        d_ref[...] = -ADAM_LR * ((mn * bc1) / (jnp.sqrt(vn * bc2) + ADAM_EPS) + ADAM_WD * w_ref[...])

    out = _sds((r, c), F32)
    return _pcall(body, name=name, out_shape=[out, out, out], grid=(r // tr, c // tc), in_specs=[spec] * 4, out_specs=[spec] * 3,
                  comm=comm)(w, g, m, v)


BIG = ("w_ffn1_in", "w_ffn1_out", "w_in", "w_pool_proj", "w_q_up", "w_kv_up", "w_mla_proj", "w_out", "w_ffn2_in", "w_ffn2_out")
ROW_SHARDED = ("w_ffn1_out", "w_out", "w_ffn2_out")
KERNEL_NAME = {"w_ffn1_in": "ffn1_in", "w_ffn1_out": "ffn1_out", "w_in": "w_in", "w_pool_proj": "pool_proj", "w_q_up": "q_up",
               "w_kv_up": "kv_up", "w_mla_proj": "mla_proj", "w_out": "w_out", "w_ffn2_in": "ffn2_in", "w_ffn2_out": "ffn2_out"}
WEIGHTS = ("w_ada", "b_ada", "norm_ffn1", "w_ffn1_in", "w_ffn1_out", "norm_mix", "w_in", "pool_grp", "pool_scale", "w_pool_proj",
           "q_a_norm", "w_q_up", "kv_a_norm", "w_kv_up", "q_norm_nope", "q_norm_rope", "k_norm_nope", "k_norm_rope", "w_mla_proj",
           "w_out", "norm_ffn2", "w_ffn2_in", "w_ffn2_out")
SMALL = ("b_ada", "norm_ffn1", "norm_mix", "pool_grp", "pool_scale", "q_a_norm", "kv_a_norm", "q_norm_nope", "q_norm_rope",
         "k_norm_nope", "k_norm_rope", "norm_ffn2")


def _assemble(name, stacked):
    if name in ROW_SHARDED:
        return stacked.reshape(stacked.shape[0] * stacked.shape[1], stacked.shape[2])
    return jnp.transpose(stacked, (1, 0, 2)).reshape(stacked.shape[1], stacked.shape[0] * stacked.shape[2])


def _split(name, full):
    if name in ROW_SHARDED:
        return full.reshape(N_CHIP, full.shape[0] // N_CHIP, full.shape[1])
    return jnp.transpose(full.reshape(full.shape[0], N_CHIP, full.shape[1] // N_CHIP), (1, 0, 2))


GU = ("w_ffn1_in", "w_ffn2_in")
NARROW = ("w_in", "w_q_up")
TRANSPOSED = ("w_in",)
W_IN_SHARD = (Z_W - LANE + ROPE) // N_CHIP


MIX_SMALL = ("w_out", "w_pool_proj", "w_mla_proj", "w_q_up", "w_kv_up")
RIDES = {
    "ada_prologue": (("gather", ("w_ffn1_in",)),),
    "ffn1_up_a": (("gather", ("w_ffn1_out",)),),
    "ffn1_up_b": (("gather", ("w_in",)),),
    "mix_in": (("gather", MIX_SMALL),),
    "attn_fwd": (("gather", ("w_ffn2_in", "w_ffn2_out")),),
    "d_ffn2_h": (("swap", ("w_ffn2_out", "w_ffn2_in")),),
    "attn_bwd": (("exchange", ("w_ffn2_out", "w_ffn2_in")),),
    "d_mix_in": (("swap", MIX_SMALL),),
    "d_mix_h": (("exchange", MIX_SMALL), ("swap", ("w_in",))),
    "ffn1_dact": (("exchange", ("w_in",)), ("swap", ("w_ffn1_out",))),
    "d_ffn1_in_a": (("exchange", ("w_ffn1_out",)),),
    "d_ffn1_in_b": (("swap", ("w_ffn1_in@0",)),),
    "d_ffn1_h": (("exchange", ("w_ffn1_in@0",)), ("swap", ("w_ffn1_in@1",))),
    "bwd_norm1": (("exchange", ("w_ffn1_in@1",)),),
    "small_allreduce": (("join", tuple(n for n in BIG if n != "w_ffn1_in") + ("w_ffn1_in@0", "w_ffn1_in@1")),),
}


def _kname(n):
    base, _, part = n.partition("@")
    return KERNEL_NAME[base] + ("@" + part if part else "")


def _both(comms):
    if len(comms) == 1:
        return comms[0]
    ins, outs, sems, aliases, spans = [], [], [], {}, []
    for c in comms:
        spans.append((len(ins), len(c.ins), len(outs), len(c.out_shapes), len(sems), len(c.sems)))
        aliases.update({len(ins) + i: len(outs) + o for i, o in c.aliases.items()})
        ins, outs, sems = ins + c.ins, outs + c.out_shapes, sems + c.sems

    def each(which):
        def run(i_, o_, s_):
            for c, (ia, ni, oa, no, sa, ns) in zip(comms, spans):
                getattr(c, which)(i_[ia:ia + ni], o_[oa:oa + no], s_[sa:sa + ns])
        return run

    return _Comm(ins, outs, sems, each("start"), each("finish"), aliases)


class _ExchangePlan:
    def __init__(self, shards, cidx):
        self.shards, self.cidx = shards, cidx
        self.W, self.G, self.parts, self.pre, self.reduced, self.joined = {}, {}, {}, {}, {}, {}

    def grad(self, key, g):
        self.G[key] = g

    def rider(self, name):
        comms = []
        for kind, names in RIDES.get(name, ()):
            if kind == "gather":
                comms.append(_gather_comm([self.shards[n] for n in names]))
            elif kind == "swap":
                for n in names:
                    self.parts[n] = self._stacked(n)
                comms.append(_swap_comm([self.parts[n] for n in names]))
            elif kind == "exchange":
                comms.append(_exchange_comm([self.pre[n] for n in names]))
            else:
                comms.append(_join_comm([self.reduced[n] for n in names]))
        return _both(comms) if comms else None

    def landed(self, name, outs):
        at = 0
        for kind, names in RIDES[name]:
            for n, o in zip(names, outs[at:at + len(names)]):
                if kind == "gather":
                    self.W[KERNEL_NAME[n]] = self._to_kernel(n, o)
                elif kind == "swap":
                    self.pre[n] = _add_half(self.parts[n], o, self.cidx, name="rs_add_" + _kname(n))
                elif kind == "exchange":
                    self.reduced[n] = _sum_chips(o, self.cidx, name="rs_sum_" + _kname(n))
                else:
                    self.joined[n] = o
            at += len(names)

    @staticmethod
    def _to_kernel(n, stacked):
        if n in GU:
            return stacked
        if n in TRANSPOSED:
            return _w_in_stacked_to_kernel(stacked, W_IN_SHARD)
        full = _assemble(n, stacked)
        return {"w_q_up": _q_up_to_kernel, "w_kv_up": _kv_up_to_kernel}.get(n, lambda w: w)(full)

    def _stacked(self, n):
        g = self.G[_kname(n)]
        if n.partition("@")[0] in GU:
            return g
        if n in TRANSPOSED:
            return _w_in_kernel_to_stacked(g, W_IN_SHARD)
        full = {"w_q_up": _q_up_from_kernel, "w_kv_up": _kv_up_from_kernel}.get(n, lambda w: w)(g)
        return _split(n, full)


def kernel(x, c, positions, w_ada, b_ada, norm_ffn1, w_ffn1_in, w_ffn1_out, norm_mix, w_in, pool_grp, pool_scale, w_pool_proj, q_a_norm, w_q_up, kv_a_norm, w_kv_up, q_norm_nope, q_norm_rope, k_norm_nope, k_norm_rope, w_mla_proj, w_out, norm_ffn2, w_ffn2_in, w_ffn2_out, loss_target, m_w_ada, m_b_ada, m_norm_ffn1, m_w_ffn1_in, m_w_ffn1_out, m_norm_mix, m_w_in, m_pool_grp, m_pool_scale, m_w_pool_proj, m_q_a_norm, m_w_q_up, m_kv_a_norm, m_w_kv_up, m_q_norm_nope, m_q_norm_rope, m_k_norm_nope, m_k_norm_rope, m_w_mla_proj, m_w_out, m_norm_ffn2, m_w_ffn2_in, m_w_ffn2_out, v_w_ada, v_b_ada, v_norm_ffn1, v_w_ffn1_in, v_w_ffn1_out, v_norm_mix, v_w_in, v_pool_grp, v_pool_scale, v_w_pool_proj, v_q_a_norm, v_w_q_up, v_kv_a_norm, v_w_kv_up, v_q_norm_nope, v_q_norm_rope, v_k_norm_nope, v_k_norm_rope, v_w_mla_proj, v_w_out, v_norm_ffn2, v_w_ffn2_in, v_w_ffn2_out):
    args = dict(locals())
    wts = {n: args[n][0] for n in WEIGHTS}
    mom = {n: args["m_" + n][0] for n in WEIGHTS}
    var = {n: args["v_" + n][0] for n in WEIGHTS}
    nb, seq, dm = x.shape
    tokens = nb * seq
    xi, yi, ci = _coords()
    chip = 2 * xi + yi

    cidx = ci.astype(jnp.int32).reshape(1)
    plan = _ExchangePlan({n: _pad_rows(wts[n].T.astype(MX)) if n in TRANSPOSED else wts[n].astype(MX) for n in BIG}, cidx)
    plan.W["pool_grp"] = wts["pool_grp"].astype(MX)

    ncol = w_ada.shape[2]
    b_cols = lax.dynamic_slice_in_dim(wts["b_ada"].reshape(1, -1), chip * ncol, ncol, axis=1)
    c_slots, mod_slots = _run(plan, _ada_prologue, c, wts["w_ada"], b_cols, name="ada_prologue")
    c_all = c_slots[:, :nb].reshape(N_DEV * nb, dm)
    mod3 = jnp.transpose(mod_slots[:, :nb], (1, 0, 2)).reshape(nb, 9, dm)
    P = {"norm_ffn1": wts["norm_ffn1"].reshape(1, dm), "norm_mix": wts["norm_mix"].reshape(1, dm),
         "norm_ffn2": wts["norm_ffn2"].reshape(1, dm), "pool_scale": wts["pool_scale"].reshape(1, POOL_W),
         "q_a_norm": wts["q_a_norm"].reshape(1, QL), "kv_a_norm": wts["kv_a_norm"].reshape(1, KVL),
         "q_gain": _gain_slab(wts["q_norm_nope"].reshape(1, NOPE), wts["q_norm_rope"].reshape(1, ROPE)),
         "k_gain": _gain_slab(wts["k_norm_nope"].reshape(1, NOPE), wts["k_norm_rope"].reshape(1, ROPE))}
    cos, sin = _rope_tables(positions.reshape(tokens))

    loss8, grad_x, stats, d_pool_grp = _layer_fwd_bwd(x.reshape(tokens, dm), loss_target.reshape(tokens, dm), mod3, cos, sin, plan, P)

    slab = _pack_stats(*stats, loss8, name="pack_stats")
    red, dmod_rows, pool_total = _run(plan, _small_allreduce, slab, d_pool_grp.reshape(POOL_G * LANE, LANE), name="small_allreduce")
    grads_t = {n: plan.joined[n][:W_IN_SHARD] for n in TRANSPOSED}
    grads = {n: grads_t[n].T if n in TRANSPOSED else plan.joined[n] for n in BIG if n != "w_ffn1_in"}
    grads["w_ffn1_in"] = jnp.concatenate([plan.joined["w_ffn1_in@0"], plan.joined["w_ffn1_in@1"]], axis=0)
    dm_rows = dmod_rows.shape[0] // N_DEV
    dmod_all = dmod_rows.reshape(N_DEV, dm_rows, SLAB_W)[:, :9 * nb].reshape(N_DEV * nb, 9 * dm)
    dmod_cols = lax.dynamic_slice_in_dim(dmod_all, chip * ncol, ncol, axis=1)
    grads["w_ada"] = _ada_bwd(c_all, dmod_cols, name="ada_bwd")

    delta, new_m, new_v = {}, {}, {}
    as2d = lambda a: a.reshape(POOL_G * LANE, LANE) if a.ndim == 4 else a.reshape(1, -1)
    upd, loss_row = _small_update(red, dmod_rows, pool_total, nb,
                                  {n: tuple(as2d(args[p + n]) for p in ("", "m_", "v_")) for n in SMALL}, name="small_update")
    loss = loss_row[0, 0]
    for n in SMALL:
        grads[n], delta[n], new_m[n], new_v[n] = upd[n]
    for n in ("w_ada",) + BIG:
        if n in NARROW:
            res = _adamw(wts[n].T, grads_t[n] if n in TRANSPOSED else grads[n].T, mom[n].T, var[n].T, name="adamw_" + n)
            delta[n], new_m[n], new_v[n] = (r.T for r in res)
        else:
            delta[n], new_m[n], new_v[n] = _adamw(wts[n], grads[n], mom[n], var[n], name="adamw_" + n)

    def lead(a, n):
        return a.reshape((1,) + wts[n].shape)

    return (loss, grad_x.reshape(nb, seq, dm), *[lead(grads[n], n) for n in WEIGHTS], *[lead(delta[n], n) for n in WEIGHTS],
            *[lead(new_m[n], n) for n in WEIGHTS], *[lead(new_v[n], n) for n in WEIGHTS])
```

```python
import functools
import math

import jax
import jax.numpy as jnp
from jax import lax
from jax.experimental import pallas as pl
from jax.experimental.pallas import tpu as pltpu

F32 = jnp.float32
MX = jnp.bfloat16

D = 1024
DFF = 2816
NH = 8
POOL_W = 512
POOL_G = 4
QL = 384
KVL = 256
ROPE = 32
NOPE = 64
LANE = 128
SUBLANE = 8
EPS = 1e-6
ATTN_SCALE = 1.0 / math.sqrt(96.0)
NEG = -1e30

Z_UP, Z_QL, Z_KV, Z_KR, Z_GP, Z_GM, Z_W = 0, 512, 896, 1152, 1280, 2304, 3328
KR_LANE = 64
LAT_W = 1280

ADAM_LR, ADAM_B1, ADAM_B2, ADAM_EPS, ADAM_WD, ADAM_STEP = 0.001, 0.9, 0.999, 1e-08, 0.01, 10

VMEM_LIMIT = 48 * 1024 * 1024
MESH = pl.DeviceIdType.MESH
N_DEV = 8
N_CHIP = 4


class _Comm:
    def __init__(self, ins, out_shapes, sems, start, finish, aliases=None):
        self.ins, self.out_shapes, self.sems = list(ins), list(out_shapes), list(sems)
        self.start, self.finish = start, finish
        self.aliases = aliases or {}


def _pcall(body, *, name, out_shape, grid=(), in_specs=None, out_specs=None, scratch=(), grid_spec=None, aliases=None,
           comm=None):
    params = pltpu.CompilerParams(vmem_limit_bytes=VMEM_LIMIT)
    kw = dict(name=name, compiler_params=params)
    if comm is None:
        if aliases:
            kw["input_output_aliases"] = aliases
        if grid_spec is not None:
            return pl.pallas_call(body, grid_spec=grid_spec, out_shape=out_shape, **kw)
        return pl.pallas_call(body, grid=grid, in_specs=in_specs, out_specs=out_specs, scratch_shapes=scratch,
                              out_shape=out_shape, **kw)
    single = not isinstance(out_shape, (list, tuple))
    outs = [out_shape] if single else list(out_shape)
    ospecs = [out_specs] if single else list(out_specs)
    n_in, n_out, n_ci, n_co, n_scr = len(in_specs), len(outs), len(comm.ins), len(comm.out_shapes), len(scratch)
    io = dict(aliases or {})
    io.update({n_in + i: n_out + o for i, o in comm.aliases.items()})

    def riding(*refs):
        ins, cins = refs[:n_in], refs[n_in:n_in + n_ci]
        at = n_in + n_ci
        os_, couts = refs[at:at + n_out], refs[at + n_out:at + n_out + n_co]
        at += n_out + n_co
        scr, csems = refs[at:at + n_scr], refs[at + n_scr:]
        if grid:
            first = functools.reduce(jnp.logical_and, [pl.program_id(d) == 0 for d in range(len(grid))])
            last = functools.reduce(jnp.logical_and, [pl.program_id(d) == grid[d] - 1 for d in range(len(grid))])
            pl.when(first)(lambda: comm.start(cins, couts, csems))
            body(*ins, *os_, *scr)
            pl.when(last)(lambda: comm.finish(cins, couts, csems))
        else:
            comm.start(cins, couts, csems)
            body(*ins, *os_, *scr)
            comm.finish(cins, couts, csems)

    call = pl.pallas_call(riding, grid=grid, in_specs=list(in_specs) + [HBM_SPEC] * n_ci, out_specs=ospecs + [HBM_SPEC] * n_co,
                          out_shape=outs + comm.out_shapes, scratch_shapes=list(scratch) + comm.sems,
                          input_output_aliases=io, **kw)

    def run(*args):
        res = call(*args, *comm.ins)
        main = list(res[:n_out])
        return (main[0] if single else main), list(res[n_out:])

    return run


def _pick(dim, target):
    best = None
    for t in range(LANE, min(dim, target) + 1, LANE):
        if dim % t == 0:
            best = t
    return dim if best is None else best


def _sds(shape, dtype):
    return jax.ShapeDtypeStruct(shape, dtype)


def _dw(a, g, *, name, tm=512, tn=1024, out_t=False, comm=None):
    return _mm(a, g, mode="tn", out_dtype=F32, name=name, tm=tm, tn=tn, tk=a.shape[0], n_outer=True, out_t=out_t, comm=comm)


def _mm(a, b, *, mode, out_dtype, name, tm=1024, tn=1024, tk=4096, n_outer=False, out_t=False, comm=None):
    if mode == "nn":
        (M, K), (K2, N) = a.shape, b.shape
    elif mode == "nt":
        (M, K), (N, K2) = a.shape, b.shape
    else:
        (K, M), (K2, N) = a.shape, b.shape
    assert K == K2, (name, a.shape, b.shape)
    tm, tn, tk = _pick(M, tm), _pick(N, tn), _pick(K, tk)
    nk = K // tk
    if n_outer:
        ij = lambda g0, g1: (g1, g0)
        grid = (N // tn, M // tm, nk)
    else:
        ij = lambda g0, g1: (g0, g1)
        grid = (M // tm, N // tn, nk)
    if mode == "tn":
        a_spec = pl.BlockSpec((tk, tm), lambda g0, g1, k: (k, ij(g0, g1)[0]))
    else:
        a_spec = pl.BlockSpec((tm, tk), lambda g0, g1, k: (ij(g0, g1)[0], k))
    if mode == "nt":
        b_spec = pl.BlockSpec((tn, tk), lambda g0, g1, k: (ij(g0, g1)[1], k))
    else:
        b_spec = pl.BlockSpec((tk, tn), lambda g0, g1, k: (k, ij(g0, g1)[1]))
    if out_t:
        assert nk == 1, name
        o_spec = pl.BlockSpec((tn, tm), lambda g0, g1, k: ij(g0, g1)[::-1])
    else:
        o_spec = pl.BlockSpec((tm, tn), lambda g0, g1, k: ij(g0, g1))
    dn = {"nn": (((1,), (0,)), ((), ())), "nt": (((1,), (1,)), ((), ())), "tn": (((0,), (0,)), ((), ()))}[mode]

    def dot(a_ref, b_ref):
        return lax.dot_general(a_ref[...].astype(MX), b_ref[...].astype(MX), dn, preferred_element_type=F32)

    def body_one(a_ref, b_ref, o_ref):
        prod = dot(a_ref, b_ref)
        o_ref[...] = (prod.T if out_t else prod).astype(o_ref.dtype)

    def body_acc(a_ref, b_ref, o_ref, acc_ref):
        k = pl.program_id(2)
        part = dot(a_ref, b_ref)

        @pl.when(k == 0)
        def _():
            acc_ref[...] = part

        @pl.when(k > 0)
        def _():
            acc_ref[...] += part

        @pl.when(k == nk - 1)
        def _():
            o_ref[...] = acc_ref[...].astype(o_ref.dtype)

    return _pcall(body_one if nk == 1 else body_acc, name=name, out_shape=_sds((N, M) if out_t else (M, N), out_dtype), grid=grid,
                  in_specs=[a_spec, b_spec], out_specs=o_spec, scratch=[] if nk == 1 else [pltpu.VMEM((tm, tn), F32)],
                  comm=comm)(a, b)


def _gu_shard(q):
    return (q % 2) * 2 + q // 2


def _ffn_up(h, w_st, *, name, tm=512, comm=None):
    T, dm = h.shape
    hw = w_st.shape[2]
    tm = _pick(T, tm)

    def body(h_ref, wg_ref, wu_ref, gu_ref, a_ref):
        hv = h_ref[...]
        g = jnp.dot(hv, wg_ref[0], preferred_element_type=F32)
        u = jnp.dot(hv, wu_ref[0], preferred_element_type=F32)
        gu_ref[:, :hw] = g.astype(gu_ref.dtype)
        gu_ref[:, hw:] = u.astype(gu_ref.dtype)
        a_ref[...] = (g * _sigmoid(g) * u).astype(a_ref.dtype)

    return _pcall(body, name=name, grid=(T // tm, 2),
                  in_specs=[pl.BlockSpec((tm, dm), lambda i, j: (i, 0)), pl.BlockSpec((1, dm, hw), lambda i, j: (j, 0, 0)),
                            pl.BlockSpec((1, dm, hw), lambda i, j: (2 + j, 0, 0))],
                  out_specs=[pl.BlockSpec((tm, 2 * hw), lambda i, j: (i, j)), pl.BlockSpec((tm, hw), lambda i, j: (i, j))],
                  out_shape=[_sds((T, 4 * hw), MX), _sds((T, 2 * hw), MX)], comm=comm)(h, w_st, w_st)


def _ffn_up_first(x, mod3, gain, w_st, *, sub, name, tm=512, comm=None):
    T, dm = x.shape
    hw = w_st.shape[2]
    tm = _pick(T, tm)
    tps = (T // mod3.shape[0]) // tm

    def body(x_ref, mod_ref, n_ref, wg_ref, wu_ref, h_ref, gu_ref, a_ref):
        xv = x_ref[...]
        xn = xv * _rsq(xv) * n_ref[...]
        hv = (xn * (1.0 + mod_ref[0, 3 * sub + 1:3 * sub + 2, :]) + mod_ref[0, 3 * sub:3 * sub + 1, :]).astype(h_ref.dtype)
        h_ref[...] = hv
        g = jnp.dot(hv, wg_ref[0], preferred_element_type=F32)
        u = jnp.dot(hv, wu_ref[0], preferred_element_type=F32)
        gu_ref[:, :hw] = g.astype(gu_ref.dtype)
        gu_ref[:, hw:] = u.astype(gu_ref.dtype)
        a_ref[...] = (g * _sigmoid(g) * u).astype(a_ref.dtype)

    return _pcall(body, name=name, grid=(T // tm,),
                  in_specs=[_row_spec(tm, dm), pl.BlockSpec((1, 9, dm), lambda i: (i // tps, 0, 0)),
                            pl.BlockSpec((1, dm), lambda i: (0, 0)), pl.BlockSpec((1, dm, hw), lambda i: (0, 0, 0)),
                            pl.BlockSpec((1, dm, hw), lambda i: (2, 0, 0))],
                  out_specs=[_row_spec(tm, dm), pl.BlockSpec((tm, 2 * hw), lambda i: (i, 0)), pl.BlockSpec((tm, hw), lambda i: (i, 0))],
                  out_shape=[_sds((T, dm), MX), _sds((T, 4 * hw), MX), _sds((T, 2 * hw), MX)],
                  comm=comm)(x, mod3, gain, w_st, w_st)


def _ffn_up_second(h, w_st, gu, a, *, name, tm=512, comm=None):
    T, dm = h.shape
    hw = w_st.shape[2]
    tm = _pick(T, tm)

    def body(h_ref, wg_ref, wu_ref, gu_in, a_in, gu_ref, a_ref):
        hv = h_ref[...]
        g = jnp.dot(hv, wg_ref[0], preferred_element_type=F32)
        u = jnp.dot(hv, wu_ref[0], preferred_element_type=F32)
        gu_ref[:, :hw] = g.astype(gu_ref.dtype)
        gu_ref[:, hw:] = u.astype(gu_ref.dtype)
        a_ref[...] = (g * _sigmoid(g) * u).astype(a_ref.dtype)

    return _pcall(body, name=name, grid=(T // tm,),
                  in_specs=[_row_spec(tm, dm), pl.BlockSpec((1, dm, hw), lambda i: (1, 0, 0)),
                            pl.BlockSpec((1, dm, hw), lambda i: (3, 0, 0)), HBM_SPEC, HBM_SPEC],
                  out_specs=[pl.BlockSpec((tm, 2 * hw), lambda i: (i, 1)), pl.BlockSpec((tm, hw), lambda i: (i, 1))],
                  out_shape=[_sds(gu.shape, gu.dtype), _sds(a.shape, a.dtype)], aliases={3: 0, 4: 1},
                  comm=comm)(h, w_st, w_st, gu, a)


def _ffn_dact(df, gu, w_out, *, name, tm=512, comm=None):
    T, dm = df.shape
    hw = gu.shape[1] // 4
    tm = _pick(T, tm)

    def body(df_ref, gu_ref, wo_ref, dgu_ref):
        da = lax.dot_general(df_ref[...], wo_ref[...], (((1,), (1,)), ((), ())), preferred_element_type=F32)
        g = gu_ref[:, :hw].astype(F32)
        u = gu_ref[:, hw:].astype(F32)
        s = _sigmoid(g)
        dgu_ref[:, :hw] = (da * u * (s * (1.0 + g * (1.0 - s)))).astype(dgu_ref.dtype)
        dgu_ref[:, hw:] = (da * (g * s)).astype(dgu_ref.dtype)

    return _pcall(body, name=name, grid=(T // tm, 2),
                  in_specs=[pl.BlockSpec((tm, dm), lambda i, j: (i, 0)), pl.BlockSpec((tm, 2 * hw), lambda i, j: (i, j)),
                            pl.BlockSpec((hw, dm), lambda i, j: (j, 0))],
                  out_specs=pl.BlockSpec((tm, 2 * hw), lambda i, j: (i, j)), out_shape=_sds(gu.shape, MX),
                  comm=comm)(df, gu, w_out)


def _ffn_dh(dgu, w_st, *, name, tm=1024, comm=None):
    T = dgu.shape[0]
    _, dm, hw = w_st.shape
    tm = _pick(T, tm)

    def body(d_ref, w_ref, o_ref, acc_ref):
        q = pl.program_id(1)
        part = lax.dot_general(d_ref[...], w_ref[0], (((1,), (1,)), ((), ())), preferred_element_type=F32)

        @pl.when(q == 0)
        def _():
            acc_ref[...] = part

        @pl.when(jnp.logical_and(q > 0, q < 3))
        def _():
            acc_ref[...] += part

        @pl.when(q == 3)
        def _():
            o_ref[...] = acc_ref[...] + part

    return _pcall(body, name=name, grid=(T // tm, 4),
                  in_specs=[pl.BlockSpec((tm, hw), lambda i, q: (i, q)), pl.BlockSpec((1, dm, hw), lambda i, q: (_gu_shard(q), 0, 0))],
                  out_specs=pl.BlockSpec((tm, dm), lambda i, q: (i, 0)), out_shape=_sds((T, dm), F32),
                  scratch=[pltpu.VMEM((tm, dm), F32)], comm=comm)(dgu, w_st)


def _ffn_dw_in(h, dgu, *, name, rows=None, tm=512, comm=None):
    T, dm = h.shape
    hw = dgu.shape[1] // 4
    first, count = rows if rows is not None else (0, dm)
    tm = _pick(count, tm)
    skip = first // tm

    def body(h_ref, d_ref, o_ref):
        o_ref[0] = lax.dot_general(h_ref[...], d_ref[...], (((0,), (0,)), ((), ())), preferred_element_type=F32)

    return _pcall(body, name=name, grid=(4, count // tm),
                  in_specs=[pl.BlockSpec((T, tm), lambda q, i: (0, skip + i)), pl.BlockSpec((T, hw), lambda q, i: (0, q))],
                  out_specs=pl.BlockSpec((1, tm, hw), lambda q, i: (_gu_shard(q), i, 0)),
                  out_shape=_sds((4, count, hw), F32), comm=comm)(h, dgu)


def _rsq(x):
    return lax.rsqrt(jnp.mean(x * x, axis=-1, keepdims=True) + EPS)


def _sigmoid(x):
    return 1.0 / (1.0 + jnp.exp(-x))


def _row_spec(tm, w):
    return pl.BlockSpec((tm, w), lambda i: (i, 0))


def _mm_resid_norm(a, w, x_prev, mod3, gain, *, sub, coef, name, tm=512, comm=None):
    T, k = a.shape
    dm = w.shape[1]
    tm = _pick(T, tm)
    tps = (T // mod3.shape[0]) // tm

    def body(a_ref, w_ref, x_ref, mod_ref, n_ref, f_ref, xo_ref, h_ref):
        f = jnp.dot(a_ref[...], w_ref[...], preferred_element_type=F32)
        f_ref[...] = f
        x = x_ref[...] + coef * mod_ref[0, 3 * sub - 1:3 * sub, :] * f
        xo_ref[...] = x
        xn = x * _rsq(x) * n_ref[...]
        h_ref[...] = (xn * (1.0 + mod_ref[0, 3 * sub + 1:3 * sub + 2, :]) + mod_ref[0, 3 * sub:3 * sub + 1, :]).astype(h_ref.dtype)

    row = _row_spec(tm, dm)
    return _pcall(body, name=name, grid=(T // tm,),
                  in_specs=[_row_spec(tm, k), pl.BlockSpec((k, dm), lambda i: (0, 0)), row,
                            pl.BlockSpec((1, 9, dm), lambda i: (i // tps, 0, 0)), pl.BlockSpec((1, dm), lambda i: (0, 0))],
                  out_specs=[row, row, row], out_shape=[_sds((T, dm), F32), _sds((T, dm), F32), _sds((T, dm), MX)],
                  comm=comm)(a, w, x_prev, mod3, gain)


def _mm_loss(a, w, x2, tgt, mod3, *, name, tm=512):
    T, k = a.shape
    dm = w.shape[1]
    tm = _pick(T, tm)
    tps = (T // mod3.shape[0]) // tm
    mod_spec = pl.BlockSpec((1, 9, dm), lambda i: (i // tps, 0, 0))
    row = _row_spec(tm, dm)
    stat_spec = pl.BlockSpec((1, SUBLANE, dm), lambda i: (i // tps, 0, 0))
    loss_spec = pl.BlockSpec((SUBLANE, LANE), lambda i: (0, 0))

    def body(a_ref, w_ref, x_ref, t_ref, mod_ref, dy_ref, df_ref, st_ref, loss_ref):
        i = pl.program_id(0)
        g = mod_ref[0, 8:9, :]
        f = jnp.dot(a_ref[...], w_ref[...], preferred_element_type=F32)
        err = x_ref[...] + 0.5 * g * f - t_ref[...]
        dy = err * (1.0 / dm)
        dy_ref[...] = dy
        df_ref[...] = (0.5 * g * dy).astype(df_ref.dtype)
        dgate = jnp.sum(0.5 * dy * f, axis=0, keepdims=True)
        part = 0.5 * jnp.sum(jnp.sum(err * err, axis=0, keepdims=True), axis=1, keepdims=True) * (1.0 / dm)

        @pl.when(i % tps == 0)
        def _():
            st_ref[...] = jnp.zeros_like(st_ref)

        @pl.when(i == 0)
        def _():
            loss_ref[...] = jnp.zeros_like(loss_ref)

        st_ref[0, 0:1, :] += dgate
        loss_ref[...] += jnp.broadcast_to(part, loss_ref.shape)

    return _pcall(body, name=name, grid=(T // tm,),
                  in_specs=[_row_spec(tm, k), pl.BlockSpec((k, dm), lambda i: (0, 0)), row, row, mod_spec],
                  out_specs=[row, row, stat_spec, loss_spec],
                  out_shape=[_sds((T, dm), F32), _sds((T, dm), MX), _sds((mod3.shape[0], SUBLANE, dm), F32),
                             _sds((SUBLANE, LANE), F32)])(a, w, x2, tgt, mod3)


def _norm_bwd_tail(dhv, x_ref, dxi_ref, f_ref, mod_ref, n_ref, dx_ref, df_ref, st_ref, *, first, sub, coef):
    x = x_ref[...]
    r = _rsq(x)
    xhat = x * r
    n = n_ref[...]
    d_shift = jnp.sum(dhv, axis=0, keepdims=True)
    d_scale = jnp.sum(dhv * (xhat * n), axis=0, keepdims=True)
    dxn = dhv * (1.0 + mod_ref[0, 3 * sub + 1:3 * sub + 2, :])
    d_gain = jnp.sum(dxn * xhat, axis=0, keepdims=True)
    dxhat = dxn * n
    dx = dxi_ref[...] + r * (dxhat - xhat * jnp.mean(dxhat * xhat, axis=-1, keepdims=True))
    dx_ref[...] = dx

    @pl.when(first)
    def _():
        st_ref[...] = jnp.zeros_like(st_ref)

    st_ref[0, 0:1, :] += d_shift
    st_ref[0, 1:2, :] += d_scale
    st_ref[0, 2:3, :] += d_gain
    if f_ref is not None:
        st_ref[0, 3:4, :] += jnp.sum(coef * dx * f_ref[...], axis=0, keepdims=True)
        df_ref[...] = (coef * mod_ref[0, 3 * sub - 1:3 * sub, :] * dx).astype(df_ref.dtype)


def _ffn_dh_norm(dgu, w_st, x_cur, dx_in, f_prev, mod3, gain, *, sub, coef, name, tm=512, comm=None):
    T = dgu.shape[0]
    _, dm, hw = w_st.shape
    tm = _pick(T, tm)
    nb = mod3.shape[0]
    tps = (T // nb) // tm

    def body(d_ref, w_ref, x_ref, dxi_ref, f_ref, mod_ref, n_ref, dx_ref, df_ref, st_ref, acc_ref):
        i = pl.program_id(0)
        q = pl.program_id(1)
        part = lax.dot_general(d_ref[...], w_ref[0], (((1,), (1,)), ((), ())), preferred_element_type=F32)

        @pl.when(q == 0)
        def _():
            acc_ref[...] = part

        @pl.when(q > 0)
        def _():
            acc_ref[...] += part

        @pl.when(q == 3)
        def _():
            _norm_bwd_tail(acc_ref[...], x_ref, dxi_ref, f_ref, mod_ref, n_ref, dx_ref, df_ref, st_ref,
                           first=i % tps == 0, sub=sub, coef=coef)

    row = pl.BlockSpec((tm, dm), lambda i, q: (i, 0))
    return _pcall(body, name=name, grid=(T // tm, 4),
                  in_specs=[pl.BlockSpec((tm, hw), lambda i, q: (i, q)), pl.BlockSpec((1, dm, hw), lambda i, q: (_gu_shard(q), 0, 0)),
                            row, row, row, pl.BlockSpec((1, 9, dm), lambda i, q: (i // tps, 0, 0)),
                            pl.BlockSpec((1, dm), lambda i, q: (0, 0))],
                  out_specs=[row, row, pl.BlockSpec((1, SUBLANE, dm), lambda i, q: (i // tps, 0, 0))],
                  out_shape=[_sds((T, dm), F32), _sds((T, dm), MX), _sds((nb, SUBLANE, dm), F32)],
                  scratch=[pltpu.VMEM((tm, dm), F32)], comm=comm)(dgu, w_st, x_cur, dx_in, f_prev, mod3, gain)


def _mm_norm_bwd(a, b, x_cur, dx_in, f_prev, mod3, gain, *, sub, coef, name, tm=512, comm=None):
    T, k = a.shape
    dm = b.shape[1]
    tm = _pick(T, tm)
    nb = mod3.shape[0]
    tps = (T // nb) // tm

    def body(a_ref, b_ref, x_ref, dxi_ref, f_ref, mod_ref, n_ref, dx_ref, df_ref, st_ref):
        dh = jnp.dot(a_ref[...], b_ref[...], preferred_element_type=F32)
        _norm_bwd_tail(dh, x_ref, dxi_ref, f_ref, mod_ref, n_ref, dx_ref, df_ref, st_ref,
                       first=pl.program_id(0) % tps == 0, sub=sub, coef=coef)

    row = _row_spec(tm, dm)
    return _pcall(body, name=name, grid=(T // tm,),
                  in_specs=[_row_spec(tm, k), pl.BlockSpec((k, dm), lambda i: (0, 0)), row, row, row,
                            pl.BlockSpec((1, 9, dm), lambda i: (i // tps, 0, 0)), pl.BlockSpec((1, dm), lambda i: (0, 0))],
                  out_specs=[row, row, pl.BlockSpec((1, SUBLANE, dm), lambda i: (i // tps, 0, 0))],
                  out_shape=[_sds((T, dm), F32), _sds((T, dm), MX), _sds((nb, SUBLANE, dm), F32)],
                  comm=comm)(a, b, x_cur, dx_in, f_prev, mod3, gain)


def _bwd_block(x_cur, dh, dx_in, f_prev, mod3, gain, *, sub, coef, name, tm=256, comm=None):
    T, dm = x_cur.shape
    nb = mod3.shape[0]
    tps = (T // nb) // tm
    has_f = f_prev is not None
    mod_spec = pl.BlockSpec((1, 9, dm), lambda i: (i // tps, 0, 0))
    vec_spec = pl.BlockSpec((1, dm), lambda i: (0, 0))
    stat_spec = pl.BlockSpec((1, SUBLANE, dm), lambda i: (i // tps, 0, 0))
    row = _row_spec(tm, dm)

    def body(*refs):
        if has_f:
            x_ref, dh_ref, dxi_ref, f_ref, mod_ref, n_ref, dx_ref, df_ref, st_ref = refs
        else:
            x_ref, dh_ref, dxi_ref, mod_ref, n_ref, dx_ref, st_ref = refs
            f_ref = df_ref = None
        _norm_bwd_tail(dh_ref[...], x_ref, dxi_ref, f_ref, mod_ref, n_ref, dx_ref, df_ref, st_ref,
                       first=pl.program_id(0) % tps == 0, sub=sub, coef=coef)

    st_shape = _sds((nb, SUBLANE, dm), F32)
    if has_f:
        return _pcall(body, name=name, grid=(T // tm,), in_specs=[row, row, row, row, mod_spec, vec_spec],
                      out_specs=[row, row, stat_spec],
                      out_shape=[_sds((T, dm), F32), _sds((T, dm), MX), st_shape], comm=comm)(x_cur, dh, dx_in, f_prev, mod3, gain)
    return _pcall(body, name=name, grid=(T // tm,), in_specs=[row, row, row, mod_spec, vec_spec],
                  out_specs=[row, stat_spec], out_shape=[_sds((T, dm), F32), st_shape], comm=comm)(x_cur, dh, dx_in, mod3, gain)


def _mix_out_fwd(z, br_pool, attn, w_mla, w_out, x_prev, mod3, gain, *, sub, name, tm=512):
    T = z.shape[0]
    dm = w_out.shape[1]
    tm = _pick(T, tm)
    tps = (T // mod3.shape[0]) // tm
    whole = lambda a: pl.BlockSpec(a.shape, lambda i: (0, 0))

    def body(z_ref, bp_ref, at_ref, wm_ref, wo_ref, x_ref, mod_ref, n_ref, bm_ref, mg_ref, mo_ref, xo_ref, h_ref):
        bm = jnp.dot(at_ref[...], wm_ref[...], preferred_element_type=F32).astype(MX)
        bm_ref[...] = bm
        gp = z_ref[:, Z_GP:Z_GP + D].astype(F32)
        gm = z_ref[:, Z_GM:Z_GM + D].astype(F32)
        merged = (_sigmoid(gp) * bp_ref[...] + _sigmoid(gm) * bm).astype(MX)
        mg_ref[...] = merged
        mo = jnp.dot(merged, wo_ref[...], preferred_element_type=F32)
        mo_ref[...] = mo
        x = x_ref[...] + mod_ref[0, 3 * sub - 1:3 * sub, :] * mo
        xo_ref[...] = x
        xn = x * _rsq(x) * n_ref[...]
        h_ref[...] = (xn * (1.0 + mod_ref[0, 3 * sub + 1:3 * sub + 2, :]) + mod_ref[0, 3 * sub:3 * sub + 1, :]).astype(h_ref.dtype)

    row = _row_spec(tm, dm)
    return _pcall(body, name=name, grid=(T // tm,),
                  in_specs=[_row_spec(tm, Z_W), row, _row_spec(tm, attn.shape[1]), whole(w_mla), whole(w_out), row,
                            pl.BlockSpec((1, 9, dm), lambda i: (i // tps, 0, 0)), pl.BlockSpec((1, dm), lambda i: (0, 0))],
                  out_specs=[row, row, row, row, row],
                  out_shape=[_sds((T, dm), MX), _sds((T, dm), MX), _sds((T, dm), F32), _sds((T, dm), F32), _sds((T, dm), MX)],
                  )(z, br_pool, attn, w_mla, w_out, x_prev, mod3, gain)


def _mix_out_bwd(dmo, w_out, z, br_pool, br_mla, w_mla, *, name, tm=512):
    T = z.shape[0]
    tm = _pick(T, tm)
    whole = lambda a: pl.BlockSpec(a.shape, lambda i: (0, 0))
    nt = (((1,), (1,)), ((), ()))

    def body(dmo_ref, wo_ref, z_ref, bp_ref, bm_ref, wm_ref, dbp_ref, dbm_ref, dg_ref, dat_ref):
        dm = lax.dot_general(dmo_ref[...], wo_ref[...], nt, preferred_element_type=F32)
        sp = _sigmoid(z_ref[:, Z_GP:Z_GP + D].astype(F32))
        sm = _sigmoid(z_ref[:, Z_GM:Z_GM + D].astype(F32))
        dbp_ref[...] = (dm * sp).astype(dbp_ref.dtype)
        dbm = (dm * sm).astype(MX)
        dbm_ref[...] = dbm
        dg_ref[:, :D] = (dm * bp_ref[...].astype(F32) * sp * (1.0 - sp)).astype(dg_ref.dtype)
        dg_ref[:, D:] = (dm * bm_ref[...].astype(F32) * sm * (1.0 - sm)).astype(dg_ref.dtype)
        dat_ref[...] = lax.dot_general(dbm, wm_ref[...], nt, preferred_element_type=F32)

    row = _row_spec(tm, D)
    return _pcall(body, name=name, grid=(T // tm,),
                  in_specs=[row, whole(w_out), _row_spec(tm, Z_W), row, row, whole(w_mla)],
                  out_specs=[row, row, _row_spec(tm, 2 * D), _row_spec(tm, w_mla.shape[0])],
                  out_shape=[_sds((T, D), MX), _sds((T, D), MX), _sds((T, 2 * D), MX), _sds((T, w_mla.shape[0]), F32)],
                  )(dmo, w_out, z, br_pool, br_mla, w_mla)


def _shift_down(x, k, row):
    return jnp.where(row >= k, pltpu.roll(x, k, 0), 0.0)


def _shift_up(x, k, row, n):
    return jnp.where(row < n - k, pltpu.roll(x, n - k, 0), 0.0)


def _pool_fwd(z, pool_grp, pool_scale, *, nb, name):
    T = z.shape[0]
    S = T // nb
    blk = pl.BlockSpec((S, LANE), lambda b, g: (b, g))

    def body(u_ref, w_ref, s_ref, pooled_ref, mixed_ref, scaled_ref):
        g = pl.program_id(1)
        u = u_ref[...].astype(F32)
        row = lax.broadcasted_iota(jnp.int32, u.shape, 0)
        s2 = u + _shift_down(u, 1, row)
        s4 = s2 + _shift_down(s2, 2, row)
        s8 = s4 + _shift_down(s4, 4, row)
        s16 = s8 + _shift_down(s8, 8, row)
        win = jnp.where(g == 0, s2, jnp.where(g == 1, s4, jnp.where(g == 2, s8, s16)))
        width = lax.shift_left(jnp.int32(2), g)
        cnt = jnp.minimum(row + 1, width).astype(F32)
        pooled = (win / cnt - u).astype(MX)
        pooled_ref[...] = pooled
        mixed = jnp.dot(pooled, w_ref[0], preferred_element_type=F32)
        mixed_ref[...] = mixed
        scaled_ref[...] = (mixed * s_ref[...]).astype(scaled_ref.dtype)

    return _pcall(body, name=name, grid=(nb, POOL_G),
                  in_specs=[blk, pl.BlockSpec((1, LANE, LANE), lambda b, g: (g, 0, 0)),
                            pl.BlockSpec((1, LANE), lambda b, g: (0, g))],
                  out_specs=[blk, blk, blk],
                  out_shape=[_sds((T, POOL_W), MX), _sds((T, POOL_W), F32), _sds((T, POOL_W), MX)])(z, pool_grp, pool_scale)


def _pool_bwd(dscaled, mixed, pooled, pool_grp, pool_scale, *, nb, name):
    T = dscaled.shape[0]
    S = T // nb
    blk = pl.BlockSpec((S, LANE), lambda g, b: (b, g))

    def body(ds_ref, mixed_ref, pooled_ref, w_ref, s_ref, du_ref, dw_ref, dsc_ref):
        g = pl.program_id(0)
        b = pl.program_id(1)
        ds = ds_ref[...]
        dsc_ref[0] = jnp.sum(ds * mixed_ref[...], axis=0, keepdims=True)
        dmixed = (ds * s_ref[...]).astype(MX)
        dw = lax.dot_general(pooled_ref[...], dmixed, (((0,), (0,)), ((), ())), preferred_element_type=F32)

        @pl.when(b == 0)
        def _():
            dw_ref[0] = dw

        @pl.when(b > 0)
        def _():
            dw_ref[0] += dw
        dpooled = lax.dot_general(dmixed, w_ref[0], (((1,), (1,)), ((), ())), preferred_element_type=F32)
        row = lax.broadcasted_iota(jnp.int32, dpooled.shape, 0)
        width = lax.shift_left(jnp.int32(2), g)
        q = dpooled / jnp.minimum(row + 1, width).astype(F32)
        r2 = q + _shift_up(q, 1, row, S)
        r4 = r2 + _shift_up(r2, 2, row, S)
        r8 = r4 + _shift_up(r4, 4, row, S)
        r16 = r8 + _shift_up(r8, 8, row, S)
        win = jnp.where(g == 0, r2, jnp.where(g == 1, r4, jnp.where(g == 2, r8, r16)))
        du_ref[...] = (win - dpooled).astype(du_ref.dtype)

    return _pcall(body, name=name, grid=(POOL_G, nb),
                  in_specs=[blk, blk, blk, pl.BlockSpec((1, LANE, LANE), lambda g, b: (g, 0, 0)),
                            pl.BlockSpec((1, LANE), lambda g, b: (0, g))],
                  out_specs=[blk, pl.BlockSpec((1, LANE, LANE), lambda g, b: (g, 0, 0)),
                             pl.BlockSpec((1, 1, LANE), lambda g, b: (b, 0, g))],
                  out_shape=[_sds((T, POOL_W), MX), _sds((POOL_G, LANE, LANE), F32), _sds((nb, 1, POOL_W), F32)],
                  )(dscaled, mixed, pooled, pool_grp, pool_scale)


def _lane_masks(shape):
    lane = lax.broadcasted_iota(jnp.int32, shape, len(shape) - 1)
    m_n = lane < NOPE
    m_r = jnp.logical_and(lane >= KR_LANE, lane < KR_LANE + ROPE)
    first_half = lane < KR_LANE + ROPE // 2
    return m_n, m_r, first_half


def _rot(y, first_half):
    return jnp.where(first_half, -pltpu.roll(y, LANE - ROPE // 2, 1), pltpu.roll(y, ROPE // 2, 1))


def _rot_t(v, first_half, m_r):
    return jnp.where(m_r, jnp.where(first_half, pltpu.roll(v, LANE - ROPE // 2, 1), -pltpu.roll(v, ROPE // 2, 1)), 0.0)


def _mla_in_fwd(z, w_q, w_kv, qa_gain, kva_gain, cos, sin, q_gain, k_gain, *, name, tm=256):
    T = z.shape[0]
    slab = pl.BlockSpec((tm, LANE), lambda i: (i, 0))
    vec = pl.BlockSpec((1, LANE), lambda i: (0, 0))
    whole = lambda a: pl.BlockSpec(a.shape, lambda i: (0, 0))

    def body(z_ref, wq_ref, wkv_ref, qa_ref, kva_ref, cos_ref, sin_ref, qg_ref, kg_ref,
             qn_ref, kvn_ref, qp_ref, kvp_ref, q_ref, k_ref, v_ref):
        ql = z_ref[:, Z_QL:Z_QL + QL].astype(F32)
        kvl = z_ref[:, Z_KV:Z_KV + KVL].astype(F32)
        qn = (ql * _rsq(ql) * qa_ref[...]).astype(MX)
        kvn = (kvl * _rsq(kvl) * kva_ref[...]).astype(MX)
        qn_ref[...] = qn
        kvn_ref[...] = kvn
        qp = jnp.dot(qn, wq_ref[...], preferred_element_type=F32)
        kvp = jnp.dot(kvn, wkv_ref[...], preferred_element_type=F32)
        qp_ref[...] = qp
        kvp_ref[...] = kvp
        m_n, m_r, first_half = _lane_masks((tm, LANE))
        c = cos_ref[...]
        s = sin_ref[...]
        qg = qg_ref[...]
        kg = kg_ref[...]
        xr = z_ref[:, Z_KR:Z_KR + LANE].astype(F32)
        rr = lax.rsqrt(jnp.sum(xr * xr, axis=-1, keepdims=True) * (1.0 / ROPE) + EPS)
        yr = xr * rr * kg
        kr = jnp.where(m_r, yr * c + _rot(yr, first_half) * s, 0.0)
        for h in range(NH):
            x = qp[:, h * LANE:(h + 1) * LANE]
            x2 = x * x
            rn = lax.rsqrt(jnp.sum(jnp.where(m_n, x2, 0.0), axis=-1, keepdims=True) * (1.0 / NOPE) + EPS)
            rq = lax.rsqrt(jnp.sum(jnp.where(m_r, x2, 0.0), axis=-1, keepdims=True) * (1.0 / ROPE) + EPS)
            y = x * jnp.where(m_n, rn, jnp.where(m_r, rq, 0.0)) * qg
            q_ref[:, h * LANE:(h + 1) * LANE] = ((y * c + _rot(y, first_half) * s) * ATTN_SCALE).astype(q_ref.dtype)
            xk = kvp[:, h * LANE:(h + 1) * LANE]
            rk = lax.rsqrt(jnp.sum(jnp.where(m_n, xk * xk, 0.0), axis=-1, keepdims=True) * (1.0 / NOPE) + EPS)
            k_ref[:, h * LANE:(h + 1) * LANE] = (jnp.where(m_n, xk * rk * kg, 0.0) + kr).astype(k_ref.dtype)
        v_ref[...] = kvp[:, NH * LANE:].astype(v_ref.dtype)

    return _pcall(body, name=name, grid=(T // tm,),
                  in_specs=[_row_spec(tm, LAT_W), whole(w_q), whole(w_kv), whole(qa_gain), whole(kva_gain), slab, slab, vec, vec],
                  out_specs=[_row_spec(tm, QL), _row_spec(tm, KVL), _row_spec(tm, NH * LANE), _row_spec(tm, NH * LANE + NH * NOPE),
                             _row_spec(tm, NH * LANE), _row_spec(tm, NH * LANE), _row_spec(tm, NH * NOPE)],
                  out_shape=[_sds((T, QL), MX), _sds((T, KVL), MX), _sds((T, NH * LANE), F32), _sds((T, NH * LANE + NH * NOPE), F32),
                             _sds((T, NH * LANE), MX), _sds((T, NH * LANE), MX), _sds((T, NH * NOPE), MX)],
                  )(z, w_q, w_kv, qa_gain, kva_gain, cos, sin, q_gain, k_gain)


def _mla_in_bwd(dq, dk, dv, qp, kvp, z, w_q, w_kv, qa_gain, kva_gain, cos, sin, q_gain, k_gain, *, nb, name, tm=256):
    T = dq.shape[0]
    tps = (T // nb) // tm
    slab = pl.BlockSpec((tm, LANE), lambda i: (i, 0))
    vec = pl.BlockSpec((1, LANE), lambda i: (0, 0))
    whole = lambda a: pl.BlockSpec(a.shape, lambda i: (0, 0))

    def latent_bwd(x, dy, gain):
        r = _rsq(x)
        xhat = x * r
        dxhat = dy * gain
        return r * (dxhat - xhat * jnp.mean(dxhat * xhat, axis=-1, keepdims=True)), jnp.sum(dy * xhat, axis=0, keepdims=True)

    def body(dq_ref, dk_ref, dv_ref, qp_ref, kvp_ref, z_ref, wq_ref, wkv_ref, qa_ref, kva_ref, cos_ref, sin_ref, qg_ref, kg_ref,
             dqp_ref, dkvp_ref, dkr_ref, dql_ref, dkvl_ref, st_ref, sq_ref, sk_ref):
        i = pl.program_id(0)
        qp, kvp = qp_ref, kvp_ref
        m_n, m_r, first_half = _lane_masks((tm, LANE))
        c = cos_ref[...]
        s = sin_ref[...]
        qg = qg_ref[...]
        kg = kg_ref[...]
        dqg = jnp.zeros((1, LANE), F32)
        dkg = jnp.zeros((1, LANE), F32)
        dkr_sum = jnp.zeros((tm, LANE), F32)
        for h in range(NH):
            x = qp[:, h * LANE:(h + 1) * LANE]
            x2 = x * x
            rn = lax.rsqrt(jnp.sum(jnp.where(m_n, x2, 0.0), axis=-1, keepdims=True) * (1.0 / NOPE) + EPS)
            rq = lax.rsqrt(jnp.sum(jnp.where(m_r, x2, 0.0), axis=-1, keepdims=True) * (1.0 / ROPE) + EPS)
            rfac = jnp.where(m_n, rn, jnp.where(m_r, rq, 0.0))
            xhat = x * rfac
            do = dq_ref[:, h * LANE:(h + 1) * LANE] * ATTN_SCALE
            dy = do * c + _rot_t(do * s, first_half, m_r)
            dqg = dqg + jnp.sum(dy * xhat, axis=0, keepdims=True)
            dxhat = dy * qg
            t = dxhat * xhat
            mean_n = jnp.sum(jnp.where(m_n, t, 0.0), axis=-1, keepdims=True) * (1.0 / NOPE)
            mean_r = jnp.sum(jnp.where(m_r, t, 0.0), axis=-1, keepdims=True) * (1.0 / ROPE)
            dqp_ref[:, h * LANE:(h + 1) * LANE] = (
                rfac * (dxhat - xhat * jnp.where(m_n, mean_n, jnp.where(m_r, mean_r, 0.0)))).astype(dqp_ref.dtype)

            xk = kvp[:, h * LANE:(h + 1) * LANE]
            rk = lax.rsqrt(jnp.sum(jnp.where(m_n, xk * xk, 0.0), axis=-1, keepdims=True) * (1.0 / NOPE) + EPS)
            khat = jnp.where(m_n, xk * rk, 0.0)
            dko = dk_ref[:, h * LANE:(h + 1) * LANE]
            dkn = jnp.where(m_n, dko, 0.0)
            dkg = dkg + jnp.sum(dkn * khat, axis=0, keepdims=True)
            dkhat = dkn * kg
            mean_k = jnp.sum(dkhat * khat, axis=-1, keepdims=True) * (1.0 / NOPE)
            dkvp_ref[:, h * LANE:(h + 1) * LANE] = jnp.where(m_n, rk * (dkhat - khat * mean_k), 0.0).astype(dkvp_ref.dtype)
            dkr_sum = dkr_sum + jnp.where(m_r, dko, 0.0)
        dkvp_ref[:, NH * LANE:] = dv_ref[...].astype(dkvp_ref.dtype)

        xr = z_ref[:, Z_KR:Z_KR + LANE].astype(F32)
        rr = lax.rsqrt(jnp.sum(xr * xr, axis=-1, keepdims=True) * (1.0 / ROPE) + EPS)
        rhat = xr * rr
        dyr = dkr_sum * c + _rot_t(dkr_sum * s, first_half, m_r)
        dkg = dkg + jnp.sum(dyr * rhat, axis=0, keepdims=True)
        drhat = dyr * kg
        mean_kr = jnp.sum(drhat * rhat, axis=-1, keepdims=True) * (1.0 / ROPE)
        dkr_ref[...] = jnp.where(m_r, rr * (drhat - rhat * mean_kr), 0.0).astype(dkr_ref.dtype)

        nt = (((1,), (1,)), ((), ()))
        dqn = lax.dot_general(dqp_ref[...], wq_ref[...], nt, preferred_element_type=F32)
        dkvn = lax.dot_general(dkvp_ref[...], wkv_ref[...], nt, preferred_element_type=F32)
        dql, dqa = latent_bwd(z_ref[:, Z_QL:Z_QL + QL].astype(F32), dqn, qa_ref[...])
        dkvl, dkva = latent_bwd(z_ref[:, Z_KV:Z_KV + KVL].astype(F32), dkvn, kva_ref[...])
        dql_ref[...] = dql.astype(dql_ref.dtype)
        dkvl_ref[...] = dkvl.astype(dkvl_ref.dtype)

        @pl.when(i % tps == 0)
        def _():
            st_ref[...] = jnp.zeros_like(st_ref)
            sq_ref[...] = jnp.zeros_like(sq_ref)
            sk_ref[...] = jnp.zeros_like(sk_ref)

        st_ref[0, 0:1, :] += dqg
        st_ref[0, 1:2, :] += dkg
        sq_ref[0, 0:1, :] += dqa
        sk_ref[0, 0:1, :] += dkva

    stat = lambda w: pl.BlockSpec((1, SUBLANE, w), lambda i: (i // tps, 0, 0))
    return _pcall(body, name=name, grid=(T // tm,),
                  in_specs=[_row_spec(tm, NH * LANE), _row_spec(tm, NH * LANE), _row_spec(tm, NH * NOPE),
                            _row_spec(tm, NH * LANE), _row_spec(tm, NH * LANE + NH * NOPE), _row_spec(tm, LAT_W),
                            whole(w_q), whole(w_kv), whole(qa_gain), whole(kva_gain), slab, slab, vec, vec],
                  out_specs=[_row_spec(tm, NH * LANE), _row_spec(tm, NH * LANE + NH * NOPE), slab, _row_spec(tm, QL),
                             _row_spec(tm, KVL), stat(LANE), stat(QL), stat(KVL)],
                  out_shape=[_sds((T, NH * LANE), MX), _sds((T, NH * LANE + NH * NOPE), MX), _sds((T, LANE), MX),
                             _sds((T, QL), MX), _sds((T, KVL), MX), _sds((nb, SUBLANE, LANE), F32),
                             _sds((nb, SUBLANE, QL), F32), _sds((nb, SUBLANE, KVL), F32)],
                  )(dq, dk, dv, qp, kvp, z, w_q, w_kv, qa_gain, kva_gain, cos, sin, q_gain, k_gain)


def _lower_triangle(t):
    return lax.broadcasted_iota(jnp.int32, (t, t), 1) <= lax.broadcasted_iota(jnp.int32, (t, t), 0)


def _attn_fwd(q, k, v, *, nb, name, tq=512, comm=None):
    T = q.shape[0]
    S = T // nb
    tq = _pick(S, tq)
    nq = S // tq
    tk = tq
    npair = NH // 2

    def body(q_ref, k_ref, v_ref, o_ref, lse_ref):
        qi = pl.program_id(2)
        lane = lax.broadcasted_iota(jnp.int32, (tq, LANE), 1)
        qs = [q_ref[:, hh * LANE:(hh + 1) * LANE] for hh in range(2)]

        def block(j, carry, diagonal):
            k0 = pl.multiple_of(j * tk, tk)
            vb = v_ref[pl.ds(k0, tk), :]
            vbs = [jnp.where(lane < NOPE, vb, jnp.ones_like(vb)), jnp.where(lane < NOPE, jnp.ones_like(vb), vb)]
            new = []
            for hh in range(2):
                m, acc = carry[hh]
                kb = k_ref[pl.ds(k0, tk), hh * LANE:(hh + 1) * LANE]
                s = lax.dot_general(qs[hh], kb, (((1,), (1,)), ((), ())), preferred_element_type=F32)
                if diagonal:
                    s = jnp.where(_lower_triangle(tq), s, NEG)
                m_new = jnp.maximum(m, jnp.max(s, axis=-1, keepdims=True))
                p = jnp.exp((s - m_new).astype(MX))
                acc = jnp.exp(m - m_new) * acc + jnp.dot(p, vbs[hh], preferred_element_type=F32)
                new.append((m_new, acc))
            return tuple(new)

        init = tuple((jnp.full((tq, 1), NEG, F32), jnp.zeros((tq, LANE), F32)) for _ in range(2))
        carry = lax.fori_loop(0, qi, lambda j, c: block(j, c, False), init)
        (m0, acc0), (m1, acc1) = block(qi, carry, True)
        l0, l1 = acc0[:, NOPE:NOPE + 1], acc1[:, 0:1]
        o_ref[...] = jnp.where(lane < NOPE, acc0 / l0, acc1 / l1).astype(o_ref.dtype)
        lse_ref[...] = jnp.where(lane < NOPE, m0 + jnp.log(l0), m1 + jnp.log(l1))

    return _pcall(body, name=name, grid=(nb, npair, nq),
                  in_specs=[pl.BlockSpec((tq, 2 * LANE), lambda b, p, i: (b * nq + i, p)),
                            pl.BlockSpec((S, 2 * LANE), lambda b, p, i: (b, p)),
                            pl.BlockSpec((S, LANE), lambda b, p, i: (b, p))],
                  out_specs=[pl.BlockSpec((tq, LANE), lambda b, p, i: (b * nq + i, p)),
                             pl.BlockSpec((tq, LANE), lambda b, p, i: (b * nq + i, p))],
                  out_shape=[_sds((T, NH * NOPE), MX), _sds((T, NH * NOPE), F32)], comm=comm)(q, k, v)


def _attn_bwd(q, k, v, o, lse, do, *, nb, name, tq=512, comm=None):
    T = q.shape[0]
    S = T // nb
    tq = _pick(S, tq)
    nq = S // tq
    tk = tq
    npair = NH // 2

    def body(q_ref, k_ref, v_ref, o_ref, lse_ref, do_ref, dq_ref, dk_ref, dv_ref, delta_ref):
        lane = lax.broadcasted_iota(jnp.int32, (tq, LANE), 1)
        first = lane < NOPE
        dq_ref[...] = jnp.zeros_like(dq_ref)

        def delta_step(qi, _):
            q0 = pl.multiple_of(qi * tq, tq)
            prod = do_ref[pl.ds(q0, tq), :] * o_ref[pl.ds(q0, tq), :].astype(F32)
            d0 = jnp.sum(jnp.where(first, prod, 0.0), axis=-1, keepdims=True)
            d1 = jnp.sum(jnp.where(first, 0.0, prod), axis=-1, keepdims=True)
            delta_ref[pl.ds(q0, tq), :] = jnp.where(first, d0, d1)
            return 0

        lax.fori_loop(0, nq, delta_step, 0)

        def kv_step(kj, _):
            k0 = pl.multiple_of(kj * tk, tk)
            kbs = [k_ref[pl.ds(k0, tk), hh * LANE:(hh + 1) * LANE] for hh in range(2)]
            vb = v_ref[pl.ds(k0, tk), :]

            def q_block(qi, carry, diagonal):
                dk0, dk1, dv = carry
                dks = [dk0, dk1]
                q0 = pl.multiple_of(qi * tq, tq)
                dov = do_ref[pl.ds(q0, tq), :]
                lse_v = lse_ref[pl.ds(q0, tq), :]
                delta_v = delta_ref[pl.ds(q0, tq), :]
                for hh in range(2):
                    qh = q_ref[pl.ds(q0, tq), hh * LANE:(hh + 1) * LANE]
                    dob = jnp.where(first if hh == 0 else jnp.logical_not(first), dov, 0.0).astype(MX)
                    s = lax.dot_general(qh, kbs[hh], (((1,), (1,)), ((), ())), preferred_element_type=F32)
                    p = jnp.exp((s - lse_v[:, hh * NOPE:hh * NOPE + 1]).astype(MX))
                    if diagonal:
                        p = jnp.where(_lower_triangle(tq), p, jnp.zeros_like(p))
                    dp = lax.dot_general(dob, vb, (((1,), (1,)), ((), ())), preferred_element_type=F32)
                    ds = p * (dp - delta_v[:, hh * NOPE:hh * NOPE + 1]).astype(MX)
                    dks[hh] = dks[hh] + lax.dot_general(ds, qh, (((0,), (0,)), ((), ())), preferred_element_type=F32)
                    dv = dv + lax.dot_general(p, dob, (((0,), (0,)), ((), ())), preferred_element_type=F32)
                    dq_ref[pl.ds(q0, tq), hh * LANE:(hh + 1) * LANE] += jnp.dot(ds, kbs[hh], preferred_element_type=F32)
                return dks[0], dks[1], dv

            zero = jnp.zeros((tk, LANE), F32)
            carry = q_block(kj, (zero, zero, zero), True)
            dk0, dk1, dv = lax.fori_loop(kj + 1, nq, lambda qi, c: q_block(qi, c, False), carry)
            dk_ref[pl.ds(k0, tk), 0:LANE] = dk0
            dk_ref[pl.ds(k0, tk), LANE:2 * LANE] = dk1
            dv_ref[pl.ds(k0, tk), :] = dv
            return 0

        lax.fori_loop(0, nq, kv_step, 0)

    pair256 = pl.BlockSpec((S, 2 * LANE), lambda b, p: (b, p))
    pair128 = pl.BlockSpec((S, LANE), lambda b, p: (b, p))
    return _pcall(body, name=name, grid=(nb, npair),
                  in_specs=[pair256, pair256, pair128, pair128, pair128, pair128],
                  out_specs=[pair256, pair256, pair128],
                  out_shape=[_sds((T, NH * LANE), F32), _sds((T, NH * LANE), F32), _sds((T, NH * NOPE), F32)],
                  scratch=[pltpu.VMEM((S, LANE), F32)], comm=comm)(q, k, v, o, lse, do)


def _run(plan, fn, *args, name, **kw):
    rider = plan.rider(name)
    if rider is None:
        return fn(*args, name=name, **kw)
    outs, landed = fn(*args, name=name, comm=rider, **kw)
    plan.landed(name, landed)
    return outs


def _layer_fwd_bwd(x, tgt, mod3, cos, sin, plan, P):
    nb = mod3.shape[0]
    W = plan.W
    h1, gu1, a1 = _run(plan, _ffn_up_first, x, mod3, P["norm_ffn1"], W["ffn1_in"], sub=0, name="ffn1_up_a")
    gu1, a1 = _run(plan, _ffn_up_second, h1, W["ffn1_in"], gu1, a1, name="ffn1_up_b")
    f1, x1, h2 = _run(plan, _mm_resid_norm, a1, W["ffn1_out"], x, mod3, P["norm_mix"], sub=1, coef=0.5, name="ffn1_out")
    z = _run(plan, _mm, h2, W["w_in"], mode="nt", out_dtype=MX, name="mix_in", tn=1664)
    pooled, mixed, scaled = _pool_fwd(z, W["pool_grp"], P["pool_scale"], nb=nb, name="pool_fwd")
    br_pool = _mm(scaled, W["pool_proj"], mode="nn", out_dtype=MX, name="pool_proj")
    qn, kvn, qp, kvp, q, k, v = _mla_in_fwd(z, W["q_up"], W["kv_up"], P["q_a_norm"], P["kv_a_norm"], cos, sin,
                                            P["q_gain"], P["k_gain"], name="mla_in_fwd")
    attn, lse = _run(plan, _attn_fwd, q, k, v, nb=nb, name="attn_fwd")
    br_mla, merged, mo, x2, h3 = _mix_out_fwd(z, br_pool, attn, W["mla_proj"], W["w_out"], x1, mod3, P["norm_ffn2"], sub=2,
                                              name="mix_out", tm=256)
    gu2, a2 = _ffn_up(h3, W["ffn2_in"], name="ffn2_up")
    dy, df2, st_fin, loss = _mm_loss(a2, W["ffn2_out"], x2, tgt, mod3, name="ffn2_out_loss")

    plan.grad("ffn2_out", _dw(a2, df2, name="d_ffn2_out"))
    dgu2 = _ffn_dact(df2, gu2, W["ffn2_out"], name="ffn2_dact")
    plan.grad("ffn2_in", _ffn_dw_in(h3, dgu2, name="d_ffn2_in"))
    dx2, dmo, st3 = _run(plan, _ffn_dh_norm, dgu2, W["ffn2_in"], x2, dy, mo, mod3, P["norm_ffn2"], sub=2, coef=1.0,
                         name="d_ffn2_h")
    plan.grad("w_out", _dw(merged, dmo, name="d_mix_out"))
    dbr_pool, dbr_mla, dgates, dattn = _mix_out_bwd(dmo, W["w_out"], z, br_pool, br_mla, W["mla_proj"], name="mix_out_bwd", tm=256)
    plan.grad("pool_proj", _dw(scaled, dbr_pool, name="d_pool_proj"))
    dscaled = _mm(dbr_pool, W["pool_proj"], mode="nt", out_dtype=F32, name="d_pool_scaled")
    du_pool, d_pool_grp, d_pool_scale = _pool_bwd(dscaled, mixed, pooled, W["pool_grp"], P["pool_scale"], nb=nb, name="pool_bwd")
    plan.grad("mla_proj", _dw(attn, dbr_mla, name="d_mla_proj"))
    dq, dk, dv = _run(plan, _attn_bwd, q, k, v, attn, lse, dattn, nb=nb, name="attn_bwd")
    dqp, dkvp, dkr, dql, dkvl, st_prep, st_q, st_kv = _mla_in_bwd(
        dq, dk, dv, qp, kvp, z, W["q_up"], W["kv_up"], P["q_a_norm"], P["kv_a_norm"], cos, sin, P["q_gain"], P["k_gain"],
        nb=nb, name="mla_in_bwd")
    plan.grad("q_up", _dw(qn, dqp, name="d_q_up"))
    plan.grad("kv_up", _dw(kvn, dkvp, name="d_kv_up"))
    dz = jnp.concatenate([du_pool, dql, dkvl, dkr, dgates], axis=1)
    plan.grad("w_in", _run(plan, _dw, h2, dz, name="d_mix_in", tm=256, tn=1664, out_t=True))
    dx1, df1, st2 = _run(plan, _mm_norm_bwd, dz, W["w_in"], x1, dx2, f1, mod3, P["norm_mix"], sub=1, coef=0.5, name="d_mix_h")
    plan.grad("ffn1_out", _run(plan, _dw, a1, df1, name="d_ffn1_out"))
    dgu1 = _run(plan, _ffn_dact, df1, gu1, W["ffn1_out"], name="ffn1_dact")
    half = x.shape[1] // 2
    plan.grad("ffn1_in@0", _run(plan, _ffn_dw_in, h1, dgu1, rows=(0, half), name="d_ffn1_in_a"))
    plan.grad("ffn1_in@1", _run(plan, _ffn_dw_in, h1, dgu1, rows=(half, half), name="d_ffn1_in_b"))
    dh1 = _run(plan, _ffn_dh, dgu1, W["ffn1_in"], name="d_ffn1_h")
    grad_x, st1 = _run(plan, _bwd_block, x, dh1, dx1, None, mod3, P["norm_ffn1"], sub=0, coef=0.0, name="bwd_norm1")

    return loss, grad_x, (st1, st2, st3, st_fin, st_q, st_kv, st_prep, d_pool_scale), d_pool_grp


def _padded(rows):
    return rows + -rows % (4 * SUBLANE)


def _pad_rows(a):
    pad = [(0, 0)] * a.ndim
    pad[-2] = (0, _padded(a.shape[-2]) - a.shape[-2])
    return jnp.pad(a, pad)


def _w_in_pieces(shard):
    runs = []
    for ref0, ref1, k0 in ((0, 1152, 0), (1152, 1184, Z_KR + KR_LANE), (1184, N_CHIP * shard, Z_GP)):
        for q in range(N_CHIP):
            lo, hi = max(ref0, q * shard), min(ref1, (q + 1) * shard)
            if lo < hi:
                runs.append((k0 + lo - ref0, q * _padded(shard) + lo - q * shard, hi - lo))
    return runs


def _w_in_stacked_to_kernel(st, shard):
    flat = st.reshape(-1, st.shape[2])
    parts, at = [], 0
    for k0, s0, n in sorted(_w_in_pieces(shard)):
        parts += [jnp.zeros((k0 - at, flat.shape[1]), st.dtype)] * (k0 > at) + [flat[s0:s0 + n]]
        at = k0 + n
    assert at == Z_W
    return jnp.concatenate(parts, axis=0)


def _w_in_kernel_to_stacked(gt, shard):
    parts, at = [], 0
    for s0, k0, n in sorted((s, k, n) for k, s, n in _w_in_pieces(shard)) + [(N_CHIP * _padded(shard), 0, 0)]:
        parts += [jnp.zeros((s0 - at, gt.shape[1]), gt.dtype)] * (s0 > at) + [gt[k0:k0 + n]] * (n > 0)
        at = s0 + n
    return jnp.concatenate(parts, axis=0).reshape(N_CHIP, _padded(shard), gt.shape[1])


def _q_up_to_kernel(w):
    k = w.shape[0]
    return jnp.pad(w.reshape(k, NH, NOPE + ROPE), ((0, 0), (0, 0), (0, LANE - NOPE - ROPE))).reshape(k, NH * LANE)


def _q_up_from_kernel(g):
    k = g.shape[0]
    return g.reshape(k, NH, LANE)[:, :, :NOPE + ROPE].reshape(k, NH * (NOPE + ROPE))


def _kv_up_to_kernel(w):
    k = w.shape[0]
    w3 = w.reshape(k, NH, 2 * NOPE)
    kpart = jnp.pad(w3[:, :, :NOPE], ((0, 0), (0, 0), (0, LANE - NOPE))).reshape(k, NH * LANE)
    return jnp.concatenate([kpart, w3[:, :, NOPE:].reshape(k, NH * NOPE)], axis=1)


def _kv_up_from_kernel(g):
    k = g.shape[0]
    kpart = g[:, :NH * LANE].reshape(k, NH, LANE)[:, :, :NOPE]
    vpart = g[:, NH * LANE:].reshape(k, NH, NOPE)
    return jnp.concatenate([kpart, vpart], axis=2).reshape(k, NH * 2 * NOPE)


def _gain_slab(nope, rope):
    return jnp.concatenate([nope, rope, jnp.zeros((1, LANE - NOPE - ROPE), nope.dtype)], axis=1)


def _rope_tables(positions):
    inv_freq = 10000.0 ** (-jnp.arange(0, ROPE, 2, dtype=F32) / ROPE)
    ang = positions.astype(F32)[:, None] * inv_freq
    ang = jnp.concatenate([ang, ang], axis=-1)
    t = positions.shape[0]
    cos = jnp.concatenate([jnp.ones((t, KR_LANE), F32), jnp.cos(ang), jnp.ones((t, LANE - KR_LANE - ROPE), F32)], axis=1)
    sin = jnp.concatenate([jnp.zeros((t, KR_LANE), F32), jnp.sin(ang), jnp.zeros((t, LANE - KR_LANE - ROPE), F32)], axis=1)
    return cos, sin


def _coords():
    return lax.axis_index("x"), lax.axis_index("y"), lax.axis_index("c")


HBM_SPEC = pl.BlockSpec(memory_space=pl.ANY)
VMEM_SPEC = pl.BlockSpec(memory_space=pltpu.VMEM)


CHIP_RELS = ((1, 0), (0, 1), (1, 1))


def _gather_comm(shards):
    n = len(shards)

    def ici(ins, outs, sems, a, j, x, y, cc):
        half = ins[a].shape[0] // 2
        mine = pl.ds(cc * half, half)
        dx, dy = CHIP_RELS[j]
        return pltpu.make_async_remote_copy(src_ref=ins[a].at[mine], dst_ref=outs[a].at[2 * x + y, mine],
                                            send_sem=sems[0].at[a, j], recv_sem=sems[1].at[a, j],
                                            device_id=(x ^ dx, y ^ dy, cc), device_id_type=MESH)

    def d2d(ins, outs, sems, a, j, x, y, cc, half_of):
        half = ins[a].shape[0] // 2
        dx, dy = CHIP_RELS[j]
        landed = outs[a].at[2 * (x ^ dx) + (y ^ dy), pl.ds(half_of * half, half)]
        return pltpu.make_async_remote_copy(src_ref=landed, dst_ref=landed, send_sem=sems[2].at[a, j], recv_sem=sems[3].at[a, j],
                                            device_id=(x, y, 1 - cc), device_id_type=MESH)

    def own(ins, outs, sems, a, x, y):
        return pltpu.make_async_copy(ins[a], outs[a].at[2 * x + y], sems[4].at[a])

    def start(ins, outs, sems):
        x, y, cc = _coords()
        for a in range(n):
            own(ins, outs, sems, a, x, y).start()
            for j in range(3):
                ici(ins, outs, sems, a, j, x, y, cc).start()

    def finish(ins, outs, sems):
        x, y, cc = _coords()
        for a in range(n):
            for j in range(3):
                ici(ins, outs, sems, a, j, x, y, cc).wait_recv()
                d2d(ins, outs, sems, a, j, x, y, cc, cc).start()
        for a in range(n):
            for j in range(3):
                d2d(ins, outs, sems, a, j, x, y, cc, 1 - cc).wait_recv()
        for a in range(n):
            for j in range(3):
                ici(ins, outs, sems, a, j, x, y, cc).wait_send()
                d2d(ins, outs, sems, a, j, x, y, cc, cc).wait_send()
            own(ins, outs, sems, a, x, y).wait()

    dma = pltpu.SemaphoreType.DMA
    return _Comm(shards, [_sds((N_CHIP,) + s.shape, s.dtype) for s in shards],
                 [dma((n, 3)), dma((n, 3)), dma((n, 3)), dma((n, 3)), dma((n,))], start, finish)


def _swap_comm(parts):
    n = len(parts)

    def copy(ins, outs, sems, a):
        x, y, cc = _coords()
        half = ins[a].shape[1] // 2
        return pltpu.make_async_remote_copy(src_ref=ins[a].at[:, pl.ds((1 - cc) * half, half)], dst_ref=outs[a],
                                            send_sem=sems[0].at[a], recv_sem=sems[1].at[a], device_id=(x, y, 1 - cc),
                                            device_id_type=MESH)

    def start(ins, outs, sems):
        for a in range(n):
            copy(ins, outs, sems, a).start()

    def finish(ins, outs, sems):
        for a in range(n):
            copy(ins, outs, sems, a).wait()

    dma = pltpu.SemaphoreType.DMA
    return _Comm(parts, [_sds((p.shape[0], p.shape[1] // 2, p.shape[2]), p.dtype) for p in parts], [dma((n,)), dma((n,))],
                 start, finish)


def _add_half(full, other, cidx, *, name):
    nch, r, c = full.shape
    half = r // 2
    tr = _pick_rows(half)
    nbk = half // tr
    grid_spec = pltpu.PrefetchScalarGridSpec(
        num_scalar_prefetch=1, grid=(nch, nbk),
        in_specs=[pl.BlockSpec((1, tr, c), lambda j, i, cref: (j, cref[0] * nbk + i, 0)),
                  pl.BlockSpec((1, tr, c), lambda j, i, cref: (j, i, 0))],
        out_specs=pl.BlockSpec((1, tr, c), lambda j, i, cref: (j, i, 0)))

    def body(cref, a_ref, b_ref, o_ref):
        o_ref[...] = (a_ref[...] + b_ref[...]).astype(o_ref.dtype)

    return _pcall(body, name=name, out_shape=_sds((nch, half, c), MX), grid_spec=grid_spec)(cidx, full, other)


def _pick_rows(rows, target=512):
    best = None
    for t in range(16, min(rows, target) + 1, 16):
        if rows % t == 0:
            best = t
    return rows if best is None else best


def _exchange_comm(parts):
    n = len(parts)

    def send(ins, outs, sems, a, j, x, y, cc):
        dx, dy = CHIP_RELS[j]
        return pltpu.make_async_remote_copy(src_ref=ins[a].at[2 * (x ^ dx) + (y ^ dy)], dst_ref=outs[a].at[2 * x + y],
                                            send_sem=sems[0].at[a, j], recv_sem=sems[1].at[a, j],
                                            device_id=(x ^ dx, y ^ dy, cc), device_id_type=MESH)

    def landing(ins, outs, sems, a, j, x, y, cc):
        dx, dy = CHIP_RELS[j]
        peer_chip = 2 * (x ^ dx) + (y ^ dy)
        return pltpu.make_async_remote_copy(src_ref=ins[a].at[peer_chip], dst_ref=outs[a].at[peer_chip], send_sem=sems[0].at[a, j],
                                            recv_sem=sems[1].at[a, j], device_id=(x, y, cc), device_id_type=MESH)

    def own(ins, outs, sems, a, x, y):
        return pltpu.make_async_copy(ins[a].at[2 * x + y], outs[a].at[2 * x + y], sems[2].at[a])

    def start(ins, outs, sems):
        x, y, cc = _coords()
        for a in range(n):
            own(ins, outs, sems, a, x, y).start()
            for j in range(3):
                send(ins, outs, sems, a, j, x, y, cc).start()

    def finish(ins, outs, sems):
        x, y, cc = _coords()
        for a in range(n):
            for j in range(3):
                landing(ins, outs, sems, a, j, x, y, cc).wait_recv()
        for a in range(n):
            for j in range(3):
                send(ins, outs, sems, a, j, x, y, cc).wait_send()
            own(ins, outs, sems, a, x, y).wait()

    dma = pltpu.SemaphoreType.DMA
    return _Comm(parts, [_sds(p.shape, p.dtype) for p in parts], [dma((n, 3)), dma((n, 3)), dma((n,))], start, finish)


def _sum_chips(q, cidx, *, name):
    nch, h, c = q.shape
    tr = _pick_rows(h)
    nbk = h // tr
    grid_spec = pltpu.PrefetchScalarGridSpec(
        num_scalar_prefetch=1, grid=(nbk,),
        in_specs=[pl.BlockSpec((nch, tr, c), lambda i, cref: (0, i, 0))],
        out_specs=pl.BlockSpec((tr, c), lambda i, cref: (cref[0] * nbk + i, 0)))

    def body(cref, q_ref, o_ref):
        acc = q_ref[0].astype(F32) + q_ref[1].astype(F32)
        acc = acc + q_ref[2].astype(F32)
        o_ref[...] = acc + q_ref[3].astype(F32)

    return _pcall(body, name=name, out_shape=_sds((2 * h, c), F32), grid_spec=grid_spec)(cidx, q)


def _join_comm(fulls):
    n = len(fulls)

    def half_copy(outs, sems, a, which):
        x, y, cc = _coords()
        h = outs[a].shape[0] // 2
        rows = outs[a].at[pl.ds((cc if which == 0 else 1 - cc) * h, h)]
        return pltpu.make_async_remote_copy(src_ref=rows, dst_ref=rows, send_sem=sems[0].at[a], recv_sem=sems[1].at[a],
                                            device_id=(x, y, 1 - cc), device_id_type=MESH)

    def start(ins, outs, sems):
        for a in range(n):
            half_copy(outs, sems, a, 0).start()

    def finish(ins, outs, sems):
        for a in range(n):
            half_copy(outs, sems, a, 1).wait_recv()
        for a in range(n):
            half_copy(outs, sems, a, 0).wait_send()

    dma = pltpu.SemaphoreType.DMA
    return _Comm(fulls, [_sds(p.shape, p.dtype) for p in fulls], [dma((n,)), dma((n,))], start, finish,
                 aliases={a: a for a in range(n)})


def _ada_prologue(c, w, b, *, name, comm=None):
    nb, dm = c.shape
    n = w.shape[1]

    def body(c_ref, w_ref, b_ref, call_ref, land_ref, cpad, mod_s, g_send, g_recv, m_send, m_recv):
        x, y, cc = _coords()
        me = 4 * x + 2 * y + cc
        chip = 2 * x + y
        cpad[...] = jnp.zeros_like(cpad)
        cpad[0:nb, :] = c_ref[...]
        copies = []
        for kk in range(1, N_DEV):
            peer = (x ^ (kk >> 2), y ^ ((kk >> 1) & 1), cc ^ (kk & 1))
            cp = pltpu.make_async_remote_copy(src_ref=cpad, dst_ref=call_ref.at[me], send_sem=g_send.at[kk - 1],
                                              recv_sem=g_recv.at[kk - 1], device_id=peer, device_id_type=MESH)
            cp.start()
            copies.append(cp)
        call_ref[me] = cpad[...]
        for kk in range(1, N_DEV):
            pltpu.make_async_remote_copy(src_ref=cpad, dst_ref=call_ref.at[me ^ kk], send_sem=g_send.at[kk - 1],
                                         recv_sem=g_recv.at[kk - 1], device_id=(x, y, cc), device_id_type=MESH).wait_recv()
        cv = call_ref[...].reshape(N_DEV * SUBLANE, dm)
        act = (cv * _sigmoid(cv)).astype(MX)
        mod = jnp.dot(act, w_ref[...].astype(MX), preferred_element_type=F32) + b_ref[...]
        mod_s[...] = mod.reshape(N_DEV, SUBLANE, n)
        for j, (dx, dy) in enumerate(CHIP_RELS):
            cp = pltpu.make_async_remote_copy(src_ref=mod_s.at[4 * (x ^ dx) + 2 * (y ^ dy) + cc], dst_ref=land_ref.at[chip],
                                              send_sem=m_send.at[j], recv_sem=m_recv.at[j],
                                              device_id=(x ^ dx, y ^ dy, cc), device_id_type=MESH)
            cp.start()
            copies.append(cp)
        land_ref[chip] = mod_s[me]
        for j, (dx, dy) in enumerate(CHIP_RELS):
            pltpu.make_async_remote_copy(src_ref=mod_s.at[me], dst_ref=land_ref.at[2 * (x ^ dx) + (y ^ dy)], send_sem=m_send.at[j],
                                         recv_sem=m_recv.at[j], device_id=(x, y, cc), device_id_type=MESH).wait_recv()
        for cp in copies:
            cp.wait_send()

    dma = pltpu.SemaphoreType.DMA
    return _pcall(body, name=name, in_specs=[VMEM_SPEC] * 3, out_specs=[VMEM_SPEC] * 2,
                  out_shape=[_sds((N_DEV, SUBLANE, dm), F32), _sds((N_CHIP, SUBLANE, n), F32)],
                  scratch=[pltpu.VMEM((SUBLANE, dm), F32), pltpu.VMEM((N_DEV, SUBLANE, n), F32),
                           dma((N_DEV - 1,)), dma((N_DEV - 1,)), dma((3,)), dma((3,))], comm=comm)(c, w, b)


def _ada_bwd(c_all, dmod_cols, *, name):
    m, kdim = c_all.shape
    n = dmod_cols.shape[1]
    tn = _pick(n, 1152)

    def body(c_ref, d_ref, o_ref):
        cv = c_ref[...]
        act = (cv * _sigmoid(cv)).astype(MX)
        o_ref[...] = lax.dot_general(act, d_ref[...].astype(MX), (((0,), (0,)), ((), ())), preferred_element_type=F32)

    return _pcall(body, name=name, out_shape=_sds((kdim, n), F32), grid=(n // tn,),
                  in_specs=[pl.BlockSpec((m, kdim), lambda j: (0, 0)), pl.BlockSpec((m, tn), lambda j: (0, j))],
                  out_specs=pl.BlockSpec((kdim, tn), lambda j: (0, j)))(c_all, dmod_cols)


SLAB_W = 1024
RED_ROWS = 8


LOSS_LANE = SLAB_W - LANE


def _pack_stats(st1, st2, st3, st_fin, st_q, st_kv, st_prep, d_pool_scale, loss8, *, name):
    nb = st1.shape[0]
    dm_rows = -(-9 * nb // SUBLANE) * SUBLANE

    def body(s1, s2, s3, sf, sq, skv, sp, sps, loss_ref, o_ref):
        o_ref[...] = jnp.zeros_like(o_ref)
        for s in range(nb):
            rows = [s1[s, 0:1, :], s1[s, 1:2, :], s2[s, 3:4, :], s2[s, 0:1, :], s2[s, 1:2, :], s3[s, 3:4, :], s3[s, 0:1, :],
                    s3[s, 1:2, :], sf[s, 0:1, :]]
            for k, row in enumerate(rows):
                o_ref[9 * s + k:9 * s + k + 1, :] = row

        def over_seq(ref, r):
            acc = ref[0, r:r + 1, :]
            for s in range(1, nb):
                acc = acc + ref[s, r:r + 1, :]
            return acc

        o_ref[dm_rows + 0:dm_rows + 1, :] = over_seq(s1, 2)
        o_ref[dm_rows + 1:dm_rows + 2, :] = over_seq(s2, 2)
        o_ref[dm_rows + 2:dm_rows + 3, :] = over_seq(s3, 2)
        o_ref[dm_rows + 3:dm_rows + 4, 0:POOL_W] = over_seq(sps, 0)
        o_ref[dm_rows + 4:dm_rows + 5, 0:QL] = over_seq(sq, 0)
        o_ref[dm_rows + 5:dm_rows + 6, 0:KVL] = over_seq(skv, 0)
        o_ref[dm_rows + 6:dm_rows + 7, 0:LANE] = over_seq(sp, 0)
        o_ref[dm_rows + 7:dm_rows + 8, 0:LANE] = over_seq(sp, 1)
        o_ref[dm_rows + 7:dm_rows + 8, LOSS_LANE:] = loss_ref[0:1, :]

    return _pcall(body, name=name, out_shape=_sds((dm_rows + RED_ROWS, SLAB_W), F32), in_specs=[VMEM_SPEC] * 9,
                  out_specs=VMEM_SPEC)(st1, st2, st3, st_fin, st_q, st_kv, st_prep, d_pool_scale, loss8)


def _small_allreduce(slab, pool, *, name, comm=None):
    rows, w = slab.shape
    dm = rows - RED_ROWS
    prow, pw = pool.shape
    crow = RED_ROWS + 2 * dm

    def body(slab_ref, pool_ref, red_ref, dmod_ref, ptot_ref, sib_slab, sib_pool, chip_slab, chip_pool, land_slab, land_pool,
             a_send, a_recv, b_send, b_recv):
        x, y, cc = _coords()
        chip = 2 * x + y
        sib = (x, y, 1 - cc)
        to_sib = [pltpu.make_async_remote_copy(src_ref=slab_ref, dst_ref=sib_slab, send_sem=a_send.at[0], recv_sem=a_recv.at[0],
                                               device_id=sib, device_id_type=MESH),
                  pltpu.make_async_remote_copy(src_ref=pool_ref, dst_ref=sib_pool, send_sem=a_send.at[1], recv_sem=a_recv.at[1],
                                               device_id=sib, device_id_type=MESH)]
        for cp in to_sib:
            cp.start()
        for cp in to_sib:
            cp.wait()
        mine_dm, theirs_dm = slab_ref[0:dm, :], sib_slab[0:dm, :]
        chip_slab[0:RED_ROWS, :] = slab_ref[dm:, :] + sib_slab[dm:, :]
        chip_slab[RED_ROWS:RED_ROWS + dm, :] = jnp.where(cc == 0, mine_dm, theirs_dm)
        chip_slab[RED_ROWS + dm:, :] = jnp.where(cc == 0, theirs_dm, mine_dm)
        chip_pool[...] = pool_ref[...] + sib_pool[...]

        sends = []
        for j, (dx, dy) in enumerate(CHIP_RELS):
            peer = (x ^ dx, y ^ dy, cc)
            sends.append(pltpu.make_async_remote_copy(src_ref=chip_slab, dst_ref=land_slab.at[chip], send_sem=b_send.at[j, 0],
                                                      recv_sem=b_recv.at[j, 0], device_id=peer, device_id_type=MESH))
            sends.append(pltpu.make_async_remote_copy(src_ref=chip_pool, dst_ref=land_pool.at[chip], send_sem=b_send.at[j, 1],
                                                      recv_sem=b_recv.at[j, 1], device_id=peer, device_id_type=MESH))
        for cp in sends:
            cp.start()
        land_slab[chip] = chip_slab[...]
        land_pool[chip] = chip_pool[...]
        for j, (dx, dy) in enumerate(CHIP_RELS):
            peer_chip = 2 * (x ^ dx) + (y ^ dy)
            pltpu.make_async_remote_copy(src_ref=chip_slab, dst_ref=land_slab.at[peer_chip], send_sem=b_send.at[j, 0],
                                         recv_sem=b_recv.at[j, 0], device_id=(x, y, cc), device_id_type=MESH).wait_recv()
            pltpu.make_async_remote_copy(src_ref=chip_pool, dst_ref=land_pool.at[peer_chip], send_sem=b_send.at[j, 1],
                                         recv_sem=b_recv.at[j, 1], device_id=(x, y, cc), device_id_type=MESH).wait_recv()
        red = land_slab[0, 0:RED_ROWS, :]
        ptot = land_pool[0]
        for ch in range(1, N_CHIP):
            red = red + land_slab[ch, 0:RED_ROWS, :]
            ptot = ptot + land_pool[ch]
        red_ref[...] = red
        ptot_ref[...] = ptot
        for ch in range(N_CHIP):
            dmod_ref[2 * dm * ch:2 * dm * (ch + 1), :] = land_slab[ch, RED_ROWS:, :]
        for cp in sends:
            cp.wait_send()

    dma = pltpu.SemaphoreType.DMA
    return _pcall(body, name=name, in_specs=[VMEM_SPEC, VMEM_SPEC], out_specs=[VMEM_SPEC] * 3,
                  out_shape=[_sds((RED_ROWS, w), F32), _sds((N_DEV * dm, w), F32), _sds((prow, pw), F32)],
                  scratch=[pltpu.VMEM((rows, w), F32), pltpu.VMEM((prow, pw), F32), pltpu.VMEM((crow, w), F32),
                           pltpu.VMEM((prow, pw), F32), pltpu.VMEM((N_CHIP, crow, w), F32), pltpu.VMEM((N_CHIP, prow, pw), F32),
                           dma((2,)), dma((2,)), dma((3, 2)), dma((3, 2))], comm=comm)(slab, pool)


def _adamw_math(w, g, m, v):
    mn = ADAM_B1 * m + (1.0 - ADAM_B1) * g
    vn = ADAM_B2 * v + (1.0 - ADAM_B2) * (g * g)
    bc1 = 1.0 / (1.0 - ADAM_B1 ** ADAM_STEP)
    bc2 = 1.0 / (1.0 - ADAM_B2 ** ADAM_STEP)
    return -ADAM_LR * ((mn * bc1) / (jnp.sqrt(vn * bc2) + ADAM_EPS) + ADAM_WD * w), mn, vn


def _small_update(red, dmod_all, pool_total, nb, params, *, name):
    names = list(SMALL)
    dm = dmod_all.shape[0] // N_DEV

    def grad_of(nm, red_ref, dmod_ref, ptot_ref):
        if nm == "b_ada":
            acc = None
            for d in range(N_DEV):
                for s in range(nb):
                    blk = dmod_ref[d * dm + 9 * s:d * dm + 9 * s + 9, :]
                    acc = blk if acc is None else acc + blk
            return jnp.concatenate([acc[k:k + 1, :] for k in range(9)], axis=1)
        if nm == "pool_grp":
            return ptot_ref[...]
        row, lo, n = {"norm_ffn1": (0, 0, D), "norm_mix": (1, 0, D), "norm_ffn2": (2, 0, D), "pool_scale": (3, 0, POOL_W),
                      "q_a_norm": (4, 0, QL), "kv_a_norm": (5, 0, KVL), "q_norm_nope": (6, 0, NOPE),
                      "q_norm_rope": (6, NOPE, ROPE), "k_norm_nope": (7, 0, NOPE), "k_norm_rope": (7, KR_LANE, ROPE)}[nm]
        return red_ref[row:row + 1, lo:lo + n]

    def body(*refs):
        red_ref, dmod_ref, ptot_ref = refs[:3]
        ins = refs[3:3 + 3 * len(names)]
        outs = refs[3 + 3 * len(names):]
        outs[4 * len(names)][...] = red_ref[RED_ROWS - 1:RED_ROWS, LOSS_LANE:]
        for i, nm in enumerate(names):
            g = grad_of(nm, red_ref, dmod_ref, ptot_ref)
            d, mn, vn = _adamw_math(ins[3 * i][...], g, ins[3 * i + 1][...], ins[3 * i + 2][...])
            outs[4 * i][...] = g
            outs[4 * i + 1][...] = d
            outs[4 * i + 2][...] = mn
            outs[4 * i + 3][...] = vn

    flat_in = [a for nm in names for a in params[nm]]
    out_shape = [_sds(params[nm][0].shape, F32) for nm in names for _ in range(4)] + [_sds((1, LANE), F32)]
    res = _pcall(body, name=name, in_specs=[VMEM_SPEC] * (3 + len(flat_in)), out_specs=[VMEM_SPEC] * len(out_shape),
                 out_shape=out_shape)(red, dmod_all, pool_total, *flat_in)
    return {nm: tuple(res[4 * i:4 * i + 4]) for i, nm in enumerate(names)}, res[-1]


def _adamw(w, g, m, v, *, name, comm=None):
    r, c = w.shape
    tr = _pick_rows(r, 256)
    tc = c if tr < r else _pick(c, 256)
    spec = pl.BlockSpec((tr, tc), lambda i, j: (i, j))
    bc1 = 1.0 / (1.0 - ADAM_B1 ** ADAM_STEP)
    bc2 = 1.0 / (1.0 - ADAM_B2 ** ADAM_STEP)

    def body(w_ref, g_ref, m_ref, v_ref, d_ref, mo_ref, vo_ref):
        gv = g_ref[...]
        mn = ADAM_B1 * m_ref[...] + (1.0 - ADAM_B1) * gv
        vn = ADAM_B2 * v_ref[...] + (1.0 - ADAM_B2) * (gv * gv)
        mo_ref[...] = mn
        vo_ref[...] = vn
        d_ref[...] = -ADAM_LR * ((mn * bc1) / (jnp.sqrt(vn * bc2) + ADAM_EPS) + ADAM_WD * w_ref[...])

    out = _sds((r, c), F32)
    return _pcall(body, name=name, out_shape=[out, out, out], grid=(r // tr, c // tc), in_specs=[spec] * 4, out_specs=[spec] * 3,
                  comm=comm)(w, g, m, v)


BIG = ("w_ffn1_in", "w_ffn1_out", "w_in", "w_pool_proj", "w_q_up", "w_kv_up", "w_mla_proj", "w_out", "w_ffn2_in", "w_ffn2_out")
ROW_SHARDED = ("w_ffn1_out", "w_out", "w_ffn2_out")
KERNEL_NAME = {"w_ffn1_in": "ffn1_in", "w_ffn1_out": "ffn1_out", "w_in": "w_in", "w_pool_proj": "pool_proj", "w_q_up": "q_up",
               "w_kv_up": "kv_up", "w_mla_proj": "mla_proj", "w_out": "w_out", "w_ffn2_in": "ffn2_in", "w_ffn2_out": "ffn2_out"}
WEIGHTS = ("w_ada", "b_ada", "norm_ffn1", "w_ffn1_in", "w_ffn1_out", "norm_mix", "w_in", "pool_grp", "pool_scale", "w_pool_proj",
           "q_a_norm", "w_q_up", "kv_a_norm", "w_kv_up", "q_norm_nope", "q_norm_rope", "k_norm_nope", "k_norm_rope", "w_mla_proj",
           "w_out", "norm_ffn2", "w_ffn2_in", "w_ffn2_out")
SMALL = ("b_ada", "norm_ffn1", "norm_mix", "pool_grp", "pool_scale", "q_a_norm", "kv_a_norm", "q_norm_nope", "q_norm_rope",
         "k_norm_nope", "k_norm_rope", "norm_ffn2")


def _assemble(name, stacked):
    if name in ROW_SHARDED:
        return stacked.reshape(stacked.shape[0] * stacked.shape[1], stacked.shape[2])
    return jnp.transpose(stacked, (1, 0, 2)).reshape(stacked.shape[1], stacked.shape[0] * stacked.shape[2])


def _split(name, full):
    if name in ROW_SHARDED:
        return full.reshape(N_CHIP, full.shape[0] // N_CHIP, full.shape[1])
    return jnp.transpose(full.reshape(full.shape[0], N_CHIP, full.shape[1] // N_CHIP), (1, 0, 2))


GU = ("w_ffn1_in", "w_ffn2_in")
NARROW = ("w_in", "w_q_up")
TRANSPOSED = ("w_in",)
W_IN_SHARD = (Z_W - LANE + ROPE) // N_CHIP


MIX_SMALL = ("w_out", "w_pool_proj", "w_mla_proj", "w_q_up", "w_kv_up")
RIDES = {
    "ada_prologue": (("gather", ("w_ffn1_in",)),),
    "ffn1_up_a": (("gather", ("w_ffn1_out",)),),
    "ffn1_up_b": (("gather", ("w_in",)),),
    "mix_in": (("gather", MIX_SMALL),),
    "attn_fwd": (("gather", ("w_ffn2_in", "w_ffn2_out")),),
    "d_ffn2_h": (("swap", ("w_ffn2_out", "w_ffn2_in")),),
    "attn_bwd": (("exchange", ("w_ffn2_out", "w_ffn2_in")),),
    "d_mix_in": (("swap", MIX_SMALL),),
    "d_mix_h": (("exchange", MIX_SMALL), ("swap", ("w_in",))),
    "ffn1_dact": (("exchange", ("w_in",)), ("swap", ("w_ffn1_out",))),
    "d_ffn1_in_a": (("exchange", ("w_ffn1_out",)),),
    "d_ffn1_in_b": (("swap", ("w_ffn1_in@0",)),),
    "d_ffn1_h": (("exchange", ("w_ffn1_in@0",)), ("swap", ("w_ffn1_in@1",))),
    "bwd_norm1": (("exchange", ("w_ffn1_in@1",)),),
    "small_allreduce": (("join", tuple(n for n in BIG if n != "w_ffn1_in") + ("w_ffn1_in@0", "w_ffn1_in@1")),),
}


def _kname(n):
    base, _, part = n.partition("@")
    return KERNEL_NAME[base] + ("@" + part if part else "")


def _both(comms):
    if len(comms) == 1:
        return comms[0]
    ins, outs, sems, aliases, spans = [], [], [], {}, []
    for c in comms:
        spans.append((len(ins), len(c.ins), len(outs), len(c.out_shapes), len(sems), len(c.sems)))
        aliases.update({len(ins) + i: len(outs) + o for i, o in c.aliases.items()})
        ins, outs, sems = ins + c.ins, outs + c.out_shapes, sems + c.sems

    def each(which):
        def run(i_, o_, s_):
            for c, (ia, ni, oa, no, sa, ns) in zip(comms, spans):
                getattr(c, which)(i_[ia:ia + ni], o_[oa:oa + no], s_[sa:sa + ns])
        return run

    return _Comm(ins, outs, sems, each("start"), each("finish"), aliases)


class _ExchangePlan:
    def __init__(self, shards, cidx):
        self.shards, self.cidx = shards, cidx
        self.W, self.G, self.parts, self.pre, self.reduced, self.joined = {}, {}, {}, {}, {}, {}

    def grad(self, key, g):
        self.G[key] = g

    def rider(self, name):
        comms = []
        for kind, names in RIDES.get(name, ()):
            if kind == "gather":
                comms.append(_gather_comm([self.shards[n] for n in names]))
            elif kind == "swap":
                for n in names:
                    self.parts[n] = self._stacked(n)
                comms.append(_swap_comm([self.parts[n] for n in names]))
            elif kind == "exchange":
                comms.append(_exchange_comm([self.pre[n] for n in names]))
            else:
                comms.append(_join_comm([self.reduced[n] for n in names]))
        return _both(comms) if comms else None

    def landed(self, name, outs):
        at = 0
        for kind, names in RIDES[name]:
            for n, o in zip(names, outs[at:at + len(names)]):
                if kind == "gather":
                    self.W[KERNEL_NAME[n]] = self._to_kernel(n, o)
                elif kind == "swap":
                    self.pre[n] = _add_half(self.parts[n], o, self.cidx, name="rs_add_" + _kname(n))
                elif kind == "exchange":
                    self.reduced[n] = _sum_chips(o, self.cidx, name="rs_sum_" + _kname(n))
                else:
                    self.joined[n] = o
            at += len(names)

    @staticmethod
    def _to_kernel(n, stacked):
        if n in GU:
            return stacked
        if n in TRANSPOSED:
            return _w_in_stacked_to_kernel(stacked, W_IN_SHARD)
        full = _assemble(n, stacked)
        return {"w_q_up": _q_up_to_kernel, "w_kv_up": _kv_up_to_kernel}.get(n, lambda w: w)(full)

    def _stacked(self, n):
        g = self.G[_kname(n)]
        if n.partition("@")[0] in GU:
            return g
        if n in TRANSPOSED:
            return _w_in_kernel_to_stacked(g, W_IN_SHARD)
        full = {"w_q_up": _q_up_from_kernel, "w_kv_up": _kv_up_from_kernel}.get(n, lambda w: w)(g)
        return _split(n, full)


def kernel(x, c, positions, w_ada, b_ada, norm_ffn1, w_ffn1_in, w_ffn1_out, norm_mix, w_in, pool_grp, pool_scale, w_pool_proj, q_a_norm, w_q_up, kv_a_norm, w_kv_up, q_norm_nope, q_norm_rope, k_norm_nope, k_norm_rope, w_mla_proj, w_out, norm_ffn2, w_ffn2_in, w_ffn2_out, loss_target, m_w_ada, m_b_ada, m_norm_ffn1, m_w_ffn1_in, m_w_ffn1_out, m_norm_mix, m_w_in, m_pool_grp, m_pool_scale, m_w_pool_proj, m_q_a_norm, m_w_q_up, m_kv_a_norm, m_w_kv_up, m_q_norm_nope, m_q_norm_rope, m_k_norm_nope, m_k_norm_rope, m_w_mla_proj, m_w_out, m_norm_ffn2, m_w_ffn2_in, m_w_ffn2_out, v_w_ada, v_b_ada, v_norm_ffn1, v_w_ffn1_in, v_w_ffn1_out, v_norm_mix, v_w_in, v_pool_grp, v_pool_scale, v_w_pool_proj, v_q_a_norm, v_w_q_up, v_kv_a_norm, v_w_kv_up, v_q_norm_nope, v_q_norm_rope, v_k_norm_nope, v_k_norm_rope, v_w_mla_proj, v_w_out, v_norm_ffn2, v_w_ffn2_in, v_w_ffn2_out):
    args = dict(locals())
    wts = {n: args[n][0] for n in WEIGHTS}
    mom = {n: args["m_" + n][0] for n in WEIGHTS}
    var = {n: args["v_" + n][0] for n in WEIGHTS}
    nb, seq, dm = x.shape
    tokens = nb * seq
    xi, yi, ci = _coords()
    chip = 2 * xi + yi

    cidx = ci.astype(jnp.int32).reshape(1)
    plan = _ExchangePlan({n: _pad_rows(wts[n].T.astype(MX)) if n in TRANSPOSED else wts[n].astype(MX) for n in BIG}, cidx)
    plan.W["pool_grp"] = wts["pool_grp"].astype(MX)

    ncol = w_ada.shape[2]
    b_cols = lax.dynamic_slice_in_dim(wts["b_ada"].reshape(1, -1), chip * ncol, ncol, axis=1)
    c_slots, mod_slots = _run(plan, _ada_prologue, c, wts["w_ada"], b_cols, name="ada_prologue")
    c_all = c_slots[:, :nb].reshape(N_DEV * nb, dm)
    mod3 = jnp.transpose(mod_slots[:, :nb], (1, 0, 2)).reshape(nb, 9, dm)
    P = {"norm_ffn1": wts["norm_ffn1"].reshape(1, dm), "norm_mix": wts["norm_mix"].reshape(1, dm),
         "norm_ffn2": wts["norm_ffn2"].reshape(1, dm), "pool_scale": wts["pool_scale"].reshape(1, POOL_W),
         "q_a_norm": wts["q_a_norm"].reshape(1, QL), "kv_a_norm": wts["kv_a_norm"].reshape(1, KVL),
         "q_gain": _gain_slab(wts["q_norm_nope"].reshape(1, NOPE), wts["q_norm_rope"].reshape(1, ROPE)),
         "k_gain": _gain_slab(wts["k_norm_nope"].reshape(1, NOPE), wts["k_norm_rope"].reshape(1, ROPE))}
    cos, sin = _rope_tables(positions.reshape(tokens))

    loss8, grad_x, stats, d_pool_grp = _layer_fwd_bwd(x.reshape(tokens, dm), loss_target.reshape(tokens, dm), mod3, cos, sin, plan, P)

    slab = _pack_stats(*stats, loss8, name="pack_stats")
    red, dmod_rows, pool_total = _run(plan, _small_allreduce, slab, d_pool_grp.reshape(POOL_G * LANE, LANE), name="small_allreduce")
    grads_t = {n: plan.joined[n][:W_IN_SHARD] for n in TRANSPOSED}
    grads = {n: grads_t[n].T if n in TRANSPOSED else plan.joined[n] for n in BIG if n != "w_ffn1_in"}
    grads["w_ffn1_in"] = jnp.concatenate([plan.joined["w_ffn1_in@0"], plan.joined["w_ffn1_in@1"]], axis=0)
    dm_rows = dmod_rows.shape[0] // N_DEV
    dmod_all = dmod_rows.reshape(N_DEV, dm_rows, SLAB_W)[:, :9 * nb].reshape(N_DEV * nb, 9 * dm)
    dmod_cols = lax.dynamic_slice_in_dim(dmod_all, chip * ncol, ncol, axis=1)
    grads["w_ada"] = _ada_bwd(c_all, dmod_cols, name="ada_bwd")

    delta, new_m, new_v = {}, {}, {}
    as2d = lambda a: a.reshape(POOL_G * LANE, LANE) if a.ndim == 4 else a.reshape(1, -1)
    upd, loss_row = _small_update(red, dmod_rows, pool_total, nb,
                                  {n: tuple(as2d(args[p + n]) for p in ("", "m_", "v_")) for n in SMALL}, name="small_update")
    loss = loss_row[0, 0]
    for n in SMALL:
        grads[n], delta[n], new_m[n], new_v[n] = upd[n]
    for n in ("w_ada",) + BIG:
        if n in NARROW:
            res = _adamw(wts[n].T, grads_t[n] if n in TRANSPOSED else grads[n].T, mom[n].T, var[n].T, name="adamw_" + n)
            delta[n], new_m[n], new_v[n] = (r.T for r in res)
        else:
            delta[n], new_m[n], new_v[n] = _adamw(wts[n], grads[n], mom[n], var[n], name="adamw_" + n)

    def lead(a, n):
        return a.reshape((1,) + wts[n].shape)

    return (loss, grad_x.reshape(nb, seq, dm), *[lead(grads[n], n) for n in WEIGHTS], *[lead(delta[n], n) for n in WEIGHTS],
            *[lead(new_m[n], n) for n in WEIGHTS], *[lead(new_v[n], n) for n in WEIGHTS])
```

```python
import functools
import math

import jax
import jax.numpy as jnp
from jax import lax
from jax.experimental import pallas as pl
from jax.experimental.pallas import tpu as pltpu

F32 = jnp.float32
MX = jnp.bfloat16

D = 1024
DFF = 2816
NH = 8
POOL_W = 512
POOL_G = 4
QL = 384
KVL = 256
ROPE = 32
NOPE = 64
LANE = 128
SUBLANE = 8
EPS = 1e-6
ATTN_SCALE = 1.0 / math.sqrt(96.0)
NEG = -1e30

Z_UP, Z_QL, Z_KV, Z_KR, Z_GP, Z_GM, Z_W = 0, 512, 896, 1152, 1280, 2304, 3328
KR_LANE = 64
LAT_W = 1280

ADAM_LR, ADAM_B1, ADAM_B2, ADAM_EPS, ADAM_WD, ADAM_STEP = 0.001, 0.9, 0.999, 1e-08, 0.01, 10

VMEM_LIMIT = 48 * 1024 * 1024
MESH = pl.DeviceIdType.MESH
N_DEV = 8
N_CHIP = 4


class _Comm:
    def __init__(self, ins, out_shapes, sems, start, finish, aliases=None):
        self.ins, self.out_shapes, self.sems = list(ins), list(out_shapes), list(sems)
        self.start, self.finish = start, finish
        self.aliases = aliases or {}


def _pcall(body, *, name, out_shape, grid=(), in_specs=None, out_specs=None, scratch=(), grid_spec=None, aliases=None,
           comm=None):
    params = pltpu.CompilerParams(vmem_limit_bytes=VMEM_LIMIT)
    kw = dict(name=name, compiler_params=params)
    if comm is None:
        if aliases:
            kw["input_output_aliases"] = aliases
        if grid_spec is not None:
            return pl.pallas_call(body, grid_spec=grid_spec, out_shape=out_shape, **kw)
        return pl.pallas_call(body, grid=grid, in_specs=in_specs, out_specs=out_specs, scratch_shapes=scratch,
                              out_shape=out_shape, **kw)
    single = not isinstance(out_shape, (list, tuple))
    outs = [out_shape] if single else list(out_shape)
    ospecs = [out_specs] if single else list(out_specs)
    n_in, n_out, n_ci, n_co, n_scr = len(in_specs), len(outs), len(comm.ins), len(comm.out_shapes), len(scratch)
    io = dict(aliases or {})
    io.update({n_in + i: n_out + o for i, o in comm.aliases.items()})

    def riding(*refs):
        ins, cins = refs[:n_in], refs[n_in:n_in + n_ci]
        at = n_in + n_ci
        os_, couts = refs[at:at + n_out], refs[at + n_out:at + n_out + n_co]
        at += n_out + n_co
        scr, csems = refs[at:at + n_scr], refs[at + n_scr:]
        if grid:
            first = functools.reduce(jnp.logical_and, [pl.program_id(d) == 0 for d in range(len(grid))])
            last = functools.reduce(jnp.logical_and, [pl.program_id(d) == grid[d] - 1 for d in range(len(grid))])
            pl.when(first)(lambda: comm.start(cins, couts, csems))
            body(*ins, *os_, *scr)
            pl.when(last)(lambda: comm.finish(cins, couts, csems))
        else:
            comm.start(cins, couts, csems)
            body(*ins, *os_, *scr)
            comm.finish(cins, couts, csems)

    call = pl.pallas_call(riding, grid=grid, in_specs=list(in_specs) + [HBM_SPEC] * n_ci, out_specs=ospecs + [HBM_SPEC] * n_co,
                          out_shape=outs + comm.out_shapes, scratch_shapes=list(scratch) + comm.sems,
                          input_output_aliases=io, **kw)

    def run(*args):
        res = call(*args, *comm.ins)
        main = list(res[:n_out])
        return (main[0] if single else main), list(res[n_out:])

    return run


def _pick(dim, target):
    best = None
    for t in range(LANE, min(dim, target) + 1, LANE):
        if dim % t == 0:
            best = t
    return dim if best is None else best


def _sds(shape, dtype):
    return jax.ShapeDtypeStruct(shape, dtype)


def _dw(a, g, *, name, tm=512, tn=1024, out_t=False, comm=None):
    return _mm(a, g, mode="tn", out_dtype=F32, name=name, tm=tm, tn=tn, tk=a.shape[0], n_outer=True, out_t=out_t, comm=comm)


def _mm(a, b, *, mode, out_dtype, name, tm=1024, tn=1024, tk=4096, n_outer=False, out_t=False, comm=None):
    if mode == "nn":
        (M, K), (K2, N) = a.shape, b.shape
    elif mode == "nt":
        (M, K), (N, K2) = a.shape, b.shape
    else:
        (K, M), (K2, N) = a.shape, b.shape
    assert K == K2, (name, a.shape, b.shape)
    tm, tn, tk = _pick(M, tm), _pick(N, tn), _pick(K, tk)
    nk = K // tk
    if n_outer:
        ij = lambda g0, g1: (g1, g0)
        grid = (N // tn, M // tm, nk)
    else:
        ij = lambda g0, g1: (g0, g1)
        grid = (M // tm, N // tn, nk)
    if mode == "tn":
        a_spec = pl.BlockSpec((tk, tm), lambda g0, g1, k: (k, ij(g0, g1)[0]))
    else:
        a_spec = pl.BlockSpec((tm, tk), lambda g0, g1, k: (ij(g0, g1)[0], k))
    if mode == "nt":
        b_spec = pl.BlockSpec((tn, tk), lambda g0, g1, k: (ij(g0, g1)[1], k))
    else:
        b_spec = pl.BlockSpec((tk, tn), lambda g0, g1, k: (k, ij(g0, g1)[1]))
    if out_t:
        assert nk == 1, name
        o_spec = pl.BlockSpec((tn, tm), lambda g0, g1, k: ij(g0, g1)[::-1])
    else:
        o_spec = pl.BlockSpec((tm, tn), lambda g0, g1, k: ij(g0, g1))
    dn = {"nn": (((1,), (0,)), ((), ())), "nt": (((1,), (1,)), ((), ())), "tn": (((0,), (0,)), ((), ()))}[mode]

    def dot(a_ref, b_ref):
        return lax.dot_general(a_ref[...].astype(MX), b_ref[...].astype(MX), dn, preferred_element_type=F32)

    def body_one(a_ref, b_ref, o_ref):
        prod = dot(a_ref, b_ref)
        o_ref[...] = (prod.T if out_t else prod).astype(o_ref.dtype)

    def body_acc(a_ref, b_ref, o_ref, acc_ref):
        k = pl.program_id(2)
        part = dot(a_ref, b_ref)

        @pl.when(k == 0)
        def _():
            acc_ref[...] = part

        @pl.when(k > 0)
        def _():
            acc_ref[...] += part

        @pl.when(k == nk - 1)
        def _():
            o_ref[...] = acc_ref[...].astype(o_ref.dtype)

    return _pcall(body_one if nk == 1 else body_acc, name=name, out_shape=_sds((N, M) if out_t else (M, N), out_dtype), grid=grid,
                  in_specs=[a_spec, b_spec], out_specs=o_spec, scratch=[] if nk == 1 else [pltpu.VMEM((tm, tn), F32)],
                  comm=comm)(a, b)


def _gu_shard(q):
    return (q % 2) * 2 + q // 2


def _ffn_up(h, w_st, *, name, tm=512, comm=None):
    T, dm = h.shape
    hw = w_st.shape[2]
    tm = _pick(T, tm)

    def body(h_ref, wg_ref, wu_ref, gu_ref, a_ref):
        hv = h_ref[...]
        g = jnp.dot(hv, wg_ref[0], preferred_element_type=F32)
        u = jnp.dot(hv, wu_ref[0], preferred_element_type=F32)
        gu_ref[:, :hw] = g.astype(gu_ref.dtype)
        gu_ref[:, hw:] = u.astype(gu_ref.dtype)
        a_ref[...] = (g * _sigmoid(g) * u).astype(a_ref.dtype)

    return _pcall(body, name=name, grid=(T // tm, 2),
                  in_specs=[pl.BlockSpec((tm, dm), lambda i, j: (i, 0)), pl.BlockSpec((1, dm, hw), lambda i, j: (j, 0, 0)),
                            pl.BlockSpec((1, dm, hw), lambda i, j: (2 + j, 0, 0))],
                  out_specs=[pl.BlockSpec((tm, 2 * hw), lambda i, j: (i, j)), pl.BlockSpec((tm, hw), lambda i, j: (i, j))],
                  out_shape=[_sds((T, 4 * hw), MX), _sds((T, 2 * hw), MX)], comm=comm)(h, w_st, w_st)


def _ffn_up_first(x, mod3, gain, w_st, *, sub, name, tm=512, comm=None):
    T, dm = x.shape
    hw = w_st.shape[2]
    tm = _pick(T, tm)
    tps = (T // mod3.shape[0]) // tm

    def body(x_ref, mod_ref, n_ref, wg_ref, wu_ref, h_ref, gu_ref, a_ref):
        xv = x_ref[...]
        xn = xv * _rsq(xv) * n_ref[...]
        hv = (xn * (1.0 + mod_ref[0, 3 * sub + 1:3 * sub + 2, :]) + mod_ref[0, 3 * sub:3 * sub + 1, :]).astype(h_ref.dtype)
        h_ref[...] = hv
        g = jnp.dot(hv, wg_ref[0], preferred_element_type=F32)
        u = jnp.dot(hv, wu_ref[0], preferred_element_type=F32)
        gu_ref[:, :hw] = g.astype(gu_ref.dtype)
        gu_ref[:, hw:] = u.astype(gu_ref.dtype)
        a_ref[...] = (g * _sigmoid(g) * u).astype(a_ref.dtype)

    return _pcall(body, name=name, grid=(T // tm,),
                  in_specs=[_row_spec(tm, dm), pl.BlockSpec((1, 9, dm), lambda i: (i // tps, 0, 0)),
                            pl.BlockSpec((1, dm), lambda i: (0, 0)), pl.BlockSpec((1, dm, hw), lambda i: (0, 0, 0)),
                            pl.BlockSpec((1, dm, hw), lambda i: (2, 0, 0))],
                  out_specs=[_row_spec(tm, dm), pl.BlockSpec((tm, 2 * hw), lambda i: (i, 0)), pl.BlockSpec((tm, hw), lambda i: (i, 0))],
                  out_shape=[_sds((T, dm), MX), _sds((T, 4 * hw), MX), _sds((T, 2 * hw), MX)],
                  comm=comm)(x, mod3, gain, w_st, w_st)


def _ffn_up_second(h, w_st, gu, a, *, name, tm=512, comm=None):
    T, dm = h.shape
    hw = w_st.shape[2]
    tm = _pick(T, tm)

    def body(h_ref, wg_ref, wu_ref, gu_in, a_in, gu_ref, a_ref):
        hv = h_ref[...]
        g = jnp.dot(hv, wg_ref[0], preferred_element_type=F32)
        u = jnp.dot(hv, wu_ref[0], preferred_element_type=F32)
        gu_ref[:, :hw] = g.astype(gu_ref.dtype)
        gu_ref[:, hw:] = u.astype(gu_ref.dtype)
        a_ref[...] = (g * _sigmoid(g) * u).astype(a_ref.dtype)

    return _pcall(body, name=name, grid=(T // tm,),
                  in_specs=[_row_spec(tm, dm), pl.BlockSpec((1, dm, hw), lambda i: (1, 0, 0)),
                            pl.BlockSpec((1, dm, hw), lambda i: (3, 0, 0)), HBM_SPEC, HBM_SPEC],
                  out_specs=[pl.BlockSpec((tm, 2 * hw), lambda i: (i, 1)), pl.BlockSpec((tm, hw), lambda i: (i, 1))],
                  out_shape=[_sds(gu.shape, gu.dtype), _sds(a.shape, a.dtype)], aliases={3: 0, 4: 1},
                  comm=comm)(h, w_st, w_st, gu, a)


def _ffn_dact(df, gu, w_out, *, name, tm=512, comm=None):
    T, dm = df.shape
    hw = gu.shape[1] // 4
    tm = _pick(T, tm)

    def body(df_ref, gu_ref, wo_ref, dgu_ref):
        da = lax.dot_general(df_ref[...], wo_ref[...], (((1,), (1,)), ((), ())), preferred_element_type=F32)
        g = gu_ref[:, :hw].astype(F32)
        u = gu_ref[:, hw:].astype(F32)
        s = _sigmoid(g)
        dgu_ref[:, :hw] = (da * u * (s * (1.0 + g * (1.0 - s)))).astype(dgu_ref.dtype)
        dgu_ref[:, hw:] = (da * (g * s)).astype(dgu_ref.dtype)

    return _pcall(body, name=name, grid=(T // tm, 2),
                  in_specs=[pl.BlockSpec((tm, dm), lambda i, j: (i, 0)), pl.BlockSpec((tm, 2 * hw), lambda i, j: (i, j)),
                            pl.BlockSpec((hw, dm), lambda i, j: (j, 0))],
                  out_specs=pl.BlockSpec((tm, 2 * hw), lambda i, j: (i, j)), out_shape=_sds(gu.shape, MX),
                  comm=comm)(df, gu, w_out)


def _ffn_dh(dgu, w_st, *, name, tm=1024, comm=None):
    T = dgu.shape[0]
    _, dm, hw = w_st.shape
    tm = _pick(T, tm)

    def body(d_ref, w_ref, o_ref, acc_ref):
        q = pl.program_id(1)
        part = lax.dot_general(d_ref[...], w_ref[0], (((1,), (1,)), ((), ())), preferred_element_type=F32)

        @pl.when(q == 0)
        def _():
            acc_ref[...] = part

        @pl.when(jnp.logical_and(q > 0, q < 3))
        def _():
            acc_ref[...] += part

        @pl.when(q == 3)
        def _():
            o_ref[...] = acc_ref[...] + part

    return _pcall(body, name=name, grid=(T // tm, 4),
                  in_specs=[pl.BlockSpec((tm, hw), lambda i, q: (i, q)), pl.BlockSpec((1, dm, hw), lambda i, q: (_gu_shard(q), 0, 0))],
                  out_specs=pl.BlockSpec((tm, dm), lambda i, q: (i, 0)), out_shape=_sds((T, dm), F32),
                  scratch=[pltpu.VMEM((tm, dm), F32)], comm=comm)(dgu, w_st)


def _ffn_dw_in(h, dgu, *, name, rows=None, tm=512, comm=None):
    T, dm = h.shape
    hw = dgu.shape[1] // 4
    first, count = rows if rows is not None else (0, dm)
    tm = _pick(count, tm)
    skip = first // tm

    def body(h_ref, d_ref, o_ref):
        o_ref[0] = lax.dot_general(h_ref[...], d_ref[...], (((0,), (0,)), ((), ())), preferred_element_type=F32)

    return _pcall(body, name=name, grid=(4, count // tm),
                  in_specs=[pl.BlockSpec((T, tm), lambda q, i: (0, skip + i)), pl.BlockSpec((T, hw), lambda q, i: (0, q))],
                  out_specs=pl.BlockSpec((1, tm, hw), lambda q, i: (_gu_shard(q), i, 0)),
                  out_shape=_sds((4, count, hw), F32), comm=comm)(h, dgu)


def _rsq(x):
    return lax.rsqrt(jnp.mean(x * x, axis=-1, keepdims=True) + EPS)


def _sigmoid(x):
    return 1.0 / (1.0 + jnp.exp(-x))


def _row_spec(tm, w):
    return pl.BlockSpec((tm, w), lambda i: (i, 0))


def _mm_resid_norm(a, w, x_prev, mod3, gain, *, sub, coef, name, tm=512, comm=None):
    T, k = a.shape
    dm = w.shape[1]
    tm = _pick(T, tm)
    tps = (T // mod3.shape[0]) // tm

    def body(a_ref, w_ref, x_ref, mod_ref, n_ref, f_ref, xo_ref, h_ref):
        f = jnp.dot(a_ref[...], w_ref[...], preferred_element_type=F32)
        f_ref[...] = f
        x = x_ref[...] + coef * mod_ref[0, 3 * sub - 1:3 * sub, :] * f
        xo_ref[...] = x
        xn = x * _rsq(x) * n_ref[...]
        h_ref[...] = (xn * (1.0 + mod_ref[0, 3 * sub + 1:3 * sub + 2, :]) + mod_ref[0, 3 * sub:3 * sub + 1, :]).astype(h_ref.dtype)

    row = _row_spec(tm, dm)
    return _pcall(body, name=name, grid=(T // tm,),
                  in_specs=[_row_spec(tm, k), pl.BlockSpec((k, dm), lambda i: (0, 0)), row,
                            pl.BlockSpec((1, 9, dm), lambda i: (i // tps, 0, 0)), pl.BlockSpec((1, dm), lambda i: (0, 0))],
                  out_specs=[row, row, row], out_shape=[_sds((T, dm), F32), _sds((T, dm), F32), _sds((T, dm), MX)],
                  comm=comm)(a, w, x_prev, mod3, gain)


def _mm_loss(a, w, x2, tgt, mod3, *, name, tm=512):
    T, k = a.shape
    dm = w.shape[1]
    tm = _pick(T, tm)
    tps = (T // mod3.shape[0]) // tm
    mod_spec = pl.BlockSpec((1, 9, dm), lambda i: (i // tps, 0, 0))
    row = _row_spec(tm, dm)
    stat_spec = pl.BlockSpec((1, SUBLANE, dm), lambda i: (i // tps, 0, 0))
    loss_spec = pl.BlockSpec((SUBLANE, LANE), lambda i: (0, 0))

    def body(a_ref, w_ref, x_ref, t_ref, mod_ref, dy_ref, df_ref, st_ref, loss_ref):
        i = pl.program_id(0)
        g = mod_ref[0, 8:9, :]
        f = jnp.dot(a_ref[...], w_ref[...], preferred_element_type=F32)
        err = x_ref[...] + 0.5 * g * f - t_ref[...]
        dy = err * (1.0 / dm)
        dy_ref[...] = dy
        df_ref[...] = (0.5 * g * dy).astype(df_ref.dtype)
        dgate = jnp.sum(0.5 * dy * f, axis=0, keepdims=True)
        part = 0.5 * jnp.sum(jnp.sum(err * err, axis=0, keepdims=True), axis=1, keepdims=True) * (1.0 / dm)

        @pl.when(i % tps == 0)
        def _():
            st_ref[...] = jnp.zeros_like(st_ref)

        @pl.when(i == 0)
        def _():
            loss_ref[...] = jnp.zeros_like(loss_ref)

        st_ref[0, 0:1, :] += dgate
        loss_ref[...] += jnp.broadcast_to(part, loss_ref.shape)

    return _pcall(body, name=name, grid=(T // tm,),
                  in_specs=[_row_spec(tm, k), pl.BlockSpec((k, dm), lambda i: (0, 0)), row, row, mod_spec],
                  out_specs=[row, row, stat_spec, loss_spec],
                  out_shape=[_sds((T, dm), F32), _sds((T, dm), MX), _sds((mod3.shape[0], SUBLANE, dm), F32),
                             _sds((SUBLANE, LANE), F32)])(a, w, x2, tgt, mod3)


def _norm_bwd_tail(dhv, x_ref, dxi_ref, f_ref, mod_ref, n_ref, dx_ref, df_ref, st_ref, *, first, sub, coef):
    x = x_ref[...]
    r = _rsq(x)
    xhat = x * r
    n = n_ref[...]
    d_shift = jnp.sum(dhv, axis=0, keepdims=True)
    d_scale = jnp.sum(dhv * (xhat * n), axis=0, keepdims=True)
    dxn = dhv * (1.0 + mod_ref[0, 3 * sub + 1:3 * sub + 2, :])
    d_gain = jnp.sum(dxn * xhat, axis=0, keepdims=True)
    dxhat = dxn * n
    dx = dxi_ref[...] + r * (dxhat - xhat * jnp.mean(dxhat * xhat, axis=-1, keepdims=True))
    dx_ref[...] = dx

    @pl.when(first)
    def _():
        st_ref[...] = jnp.zeros_like(st_ref)

    st_ref[0, 0:1, :] += d_shift
    st_ref[0, 1:2, :] += d_scale
    st_ref[0, 2:3, :] += d_gain
    if f_ref is not None:
        st_ref[0, 3:4, :] += jnp.sum(coef * dx * f_ref[...], axis=0, keepdims=True)
        df_ref[...] = (coef * mod_ref[0, 3 * sub - 1:3 * sub, :] * dx).astype(df_ref.dtype)


def _ffn_dh_norm(dgu, w_st, x_cur, dx_in, f_prev, mod3, gain, *, sub, coef, name, tm=512, comm=None):
    T = dgu.shape[0]
    _, dm, hw = w_st.shape
    tm = _pick(T, tm)
    nb = mod3.shape[0]
    tps = (T // nb) // tm

    def body(d_ref, w_ref, x_ref, dxi_ref, f_ref, mod_ref, n_ref, dx_ref, df_ref, st_ref, acc_ref):
        i = pl.program_id(0)
        q = pl.program_id(1)
        part = lax.dot_general(d_ref[...], w_ref[0], (((1,), (1,)), ((), ())), preferred_element_type=F32)

        @pl.when(q == 0)
        def _():
            acc_ref[...] = part

        @pl.when(q > 0)
        def _():
            acc_ref[...] += part

        @pl.when(q == 3)
        def _():
            _norm_bwd_tail(acc_ref[...], x_ref, dxi_ref, f_ref, mod_ref, n_ref, dx_ref, df_ref, st_ref,
                           first=i % tps == 0, sub=sub, coef=coef)

    row = pl.BlockSpec((tm, dm), lambda i, q: (i, 0))
    return _pcall(body, name=name, grid=(T // tm, 4),
                  in_specs=[pl.BlockSpec((tm, hw), lambda i, q: (i, q)), pl.BlockSpec((1, dm, hw), lambda i, q: (_gu_shard(q), 0, 0)),
                            row, row, row, pl.BlockSpec((1, 9, dm), lambda i, q: (i // tps, 0, 0)),
                            pl.BlockSpec((1, dm), lambda i, q: (0, 0))],
                  out_specs=[row, row, pl.BlockSpec((1, SUBLANE, dm), lambda i, q: (i // tps, 0, 0))],
                  out_shape=[_sds((T, dm), F32), _sds((T, dm), MX), _sds((nb, SUBLANE, dm), F32)],
                  scratch=[pltpu.VMEM((tm, dm), F32)], comm=comm)(dgu, w_st, x_cur, dx_in, f_prev, mod3, gain)


def _mm_norm_bwd(a, b, x_cur, dx_in, f_prev, mod3, gain, *, sub, coef, name, tm=512, comm=None):
    T, k = a.shape
    dm = b.shape[1]
    tm = _pick(T, tm)
    nb = mod3.shape[0]
    tps = (T // nb) // tm

    def body(a_ref, b_ref, x_ref, dxi_ref, f_ref, mod_ref, n_ref, dx_ref, df_ref, st_ref):
        dh = jnp.dot(a_ref[...], b_ref[...], preferred_element_type=F32)
        _norm_bwd_tail(dh, x_ref, dxi_ref, f_ref, mod_ref, n_ref, dx_ref, df_ref, st_ref,
                       first=pl.program_id(0) % tps == 0, sub=sub, coef=coef)

    row = _row_spec(tm, dm)
    return _pcall(body, name=name, grid=(T // tm,),
                  in_specs=[_row_spec(tm, k), pl.BlockSpec((k, dm), lambda i: (0, 0)), row, row, row,
                            pl.BlockSpec((1, 9, dm), lambda i: (i // tps, 0, 0)), pl.BlockSpec((1, dm), lambda i: (0, 0))],
                  out_specs=[row, row, pl.BlockSpec((1, SUBLANE, dm), lambda i: (i // tps, 0, 0))],
                  out_shape=[_sds((T, dm), F32), _sds((T, dm), MX), _sds((nb, SUBLANE, dm), F32)],
                  comm=comm)(a, b, x_cur, dx_in, f_prev, mod3, gain)


def _bwd_block(x_cur, dh, dx_in, f_prev, mod3, gain, *, sub, coef, name, tm=256, comm=None):
    T, dm = x_cur.shape
    nb = mod3.shape[0]
    tps = (T // nb) // tm
    has_f = f_prev is not None
    mod_spec = pl.BlockSpec((1, 9, dm), lambda i: (i // tps, 0, 0))
    vec_spec = pl.BlockSpec((1, dm), lambda i: (0, 0))
    stat_spec = pl.BlockSpec((1, SUBLANE, dm), lambda i: (i // tps, 0, 0))
    row = _row_spec(tm, dm)

    def body(*refs):
        if has_f:
            x_ref, dh_ref, dxi_ref, f_ref, mod_ref, n_ref, dx_ref, df_ref, st_ref = refs
        else:
            x_ref, dh_ref, dxi_ref, mod_ref, n_ref, dx_ref, st_ref = refs
            f_ref = df_ref = None
        _norm_bwd_tail(dh_ref[...], x_ref, dxi_ref, f_ref, mod_ref, n_ref, dx_ref, df_ref, st_ref,
                       first=pl.program_id(0) % tps == 0, sub=sub, coef=coef)

    st_shape = _sds((nb, SUBLANE, dm), F32)
    if has_f:
        return _pcall(body, name=name, grid=(T // tm,), in_specs=[row, row, row, row, mod_spec, vec_spec],
                      out_specs=[row, row, stat_spec],
                      out_shape=[_sds((T, dm), F32), _sds((T, dm), MX), st_shape], comm=comm)(x_cur, dh, dx_in, f_prev, mod3, gain)
    return _pcall(body, name=name, grid=(T // tm,), in_specs=[row, row, row, mod_spec, vec_spec],
                  out_specs=[row, stat_spec], out_shape=[_sds((T, dm), F32), st_shape], comm=comm)(x_cur, dh, dx_in, mod3, gain)


def _mix_out_fwd(z, br_pool, attn, w_mla, w_out, x_prev, mod3, gain, *, sub, name, tm=512):
    T = z.shape[0]
    dm = w_out.shape[1]
    tm = _pick(T, tm)
    tps = (T // mod3.shape[0]) // tm
    whole = lambda a: pl.BlockSpec(a.shape, lambda i: (0, 0))

    def body(z_ref, bp_ref, at_ref, wm_ref, wo_ref, x_ref, mod_ref, n_ref, bm_ref, mg_ref, mo_ref, xo_ref, h_ref):
        bm = jnp.dot(at_ref[...], wm_ref[...], preferred_element_type=F32).astype(MX)
        bm_ref[...] = bm
        gp = z_ref[:, Z_GP:Z_GP + D].astype(F32)
        gm = z_ref[:, Z_GM:Z_GM + D].astype(F32)
        merged = (_sigmoid(gp) * bp_ref[...] + _sigmoid(gm) * bm).astype(MX)
        mg_ref[...] = merged
        mo = jnp.dot(merged, wo_ref[...], preferred_element_type=F32)
        mo_ref[...] = mo
        x = x_ref[...] + mod_ref[0, 3 * sub - 1:3 * sub, :] * mo
        xo_ref[...] = x
        xn = x * _rsq(x) * n_ref[...]
        h_ref[...] = (xn * (1.0 + mod_ref[0, 3 * sub + 1:3 * sub + 2, :]) + mod_ref[0, 3 * sub:3 * sub + 1, :]).astype(h_ref.dtype)

    row = _row_spec(tm, dm)
    return _pcall(body, name=name, grid=(T // tm,),
                  in_specs=[_row_spec(tm, Z_W), row, _row_spec(tm, attn.shape[1]), whole(w_mla), whole(w_out), row,
                            pl.BlockSpec((1, 9, dm), lambda i: (i // tps, 0, 0)), pl.BlockSpec((1, dm), lambda i: (0, 0))],
                  out_specs=[row, row, row, row, row],
                  out_shape=[_sds((T, dm), MX), _sds((T, dm), MX), _sds((T, dm), F32), _sds((T, dm), F32), _sds((T, dm), MX)],
                  )(z, br_pool, attn, w_mla, w_out, x_prev, mod3, gain)


def _mix_out_bwd(dmo, w_out, z, br_pool, br_mla, w_mla, *, name, tm=512):
    T = z.shape[0]
    tm = _pick(T, tm)
    whole = lambda a: pl.BlockSpec(a.shape, lambda i: (0, 0))
    nt = (((1,), (1,)), ((), ()))

    def body(dmo_ref, wo_ref, z_ref, bp_ref, bm_ref, wm_ref, dbp_ref, dbm_ref, dg_ref, dat_ref):
        dm = lax.dot_general(dmo_ref[...], wo_ref[...], nt, preferred_element_type=F32)
        sp = _sigmoid(z_ref[:, Z_GP:Z_GP + D].astype(F32))
        sm = _sigmoid(z_ref[:, Z_GM:Z_GM + D].astype(F32))
        dbp_ref[...] = (dm * sp).astype(dbp_ref.dtype)
        dbm = (dm * sm).astype(MX)
        dbm_ref[...] = dbm
        dg_ref[:, :D] = (dm * bp_ref[...].astype(F32) * sp * (1.0 - sp)).astype(dg_ref.dtype)
        dg_ref[:, D:] = (dm * bm_ref[...].astype(F32) * sm * (1.0 - sm)).astype(dg_ref.dtype)
        dat_ref[...] = lax.dot_general(dbm, wm_ref[...], nt, preferred_element_type=F32)

    row = _row_spec(tm, D)
    return _pcall(body, name=name, grid=(T // tm,),
                  in_specs=[row, whole(w_out), _row_spec(tm, Z_W), row, row, whole(w_mla)],
                  out_specs=[row, row, _row_spec(tm, 2 * D), _row_spec(tm, w_mla.shape[0])],
                  out_shape=[_sds((T, D), MX), _sds((T, D), MX), _sds((T, 2 * D), MX), _sds((T, w_mla.shape[0]), F32)],
                  )(dmo, w_out, z, br_pool, br_mla, w_mla)


def _shift_down(x, k, row):
    return jnp.where(row >= k, pltpu.roll(x, k, 0), 0.0)


def _shift_up(x, k, row, n):
    return jnp.where(row < n - k, pltpu.roll(x, n - k, 0), 0.0)


def _pool_fwd(z, pool_grp, pool_scale, *, nb, name):
    T = z.shape[0]
    S = T // nb
    blk = pl.BlockSpec((S, LANE), lambda b, g: (b, g))

    def body(u_ref, w_ref, s_ref, pooled_ref, mixed_ref, scaled_ref):
        g = pl.program_id(1)
        u = u_ref[...].astype(F32)
        row = lax.broadcasted_iota(jnp.int32, u.shape, 0)
        s2 = u + _shift_down(u, 1, row)
        s4 = s2 + _shift_down(s2, 2, row)
        s8 = s4 + _shift_down(s4, 4, row)
        s16 = s8 + _shift_down(s8, 8, row)
        win = jnp.where(g == 0, s2, jnp.where(g == 1, s4, jnp.where(g == 2, s8, s16)))
        width = lax.shift_left(jnp.int32(2), g)
        cnt = jnp.minimum(row + 1, width).astype(F32)
        pooled = (win / cnt - u).astype(MX)
        pooled_ref[...] = pooled
        mixed = jnp.dot(pooled, w_ref[0], preferred_element_type=F32)
        mixed_ref[...] = mixed
        scaled_ref[...] = (mixed * s_ref[...]).astype(scaled_ref.dtype)

    return _pcall(body, name=name, grid=(nb, POOL_G),
                  in_specs=[blk, pl.BlockSpec((1, LANE, LANE), lambda b, g: (g, 0, 0)),
                            pl.BlockSpec((1, LANE), lambda b, g: (0, g))],
                  out_specs=[blk, blk, blk],
                  out_shape=[_sds((T, POOL_W), MX), _sds((T, POOL_W), F32), _sds((T, POOL_W), MX)])(z, pool_grp, pool_scale)


def _pool_bwd(dscaled, mixed, pooled, pool_grp, pool_scale, *, nb, name):
    T = dscaled.shape[0]
    S = T // nb
    blk = pl.BlockSpec((S, LANE), lambda g, b: (b, g))

    def body(ds_ref, mixed_ref, pooled_ref, w_ref, s_ref, du_ref, dw_ref, dsc_ref):
        g = pl.program_id(0)
        b = pl.program_id(1)
        ds = ds_ref[...]
        dsc_ref[0] = jnp.sum(ds * mixed_ref[...], axis=0, keepdims=True)
        dmixed = (ds * s_ref[...]).astype(MX)
        dw = lax.dot_general(pooled_ref[...], dmixed, (((0,), (0,)), ((), ())), preferred_element_type=F32)

        @pl.when(b == 0)
        def _():
            dw_ref[0] = dw

        @pl.when(b > 0)
        def _():
            dw_ref[0] += dw
        dpooled = lax.dot_general(dmixed, w_ref[0], (((1,), (1,)), ((), ())), preferred_element_type=F32)
        row = lax.broadcasted_iota(jnp.int32, dpooled.shape, 0)
        width = lax.shift_left(jnp.int32(2), g)
        q = dpooled / jnp.minimum(row + 1, width).astype(F32)
        r2 = q + _shift_up(q, 1, row, S)
        r4 = r2 + _shift_up(r2, 2, row, S)
        r8 = r4 + _shift_up(r4, 4, row, S)
        r16 = r8 + _shift_up(r8, 8, row, S)
        win = jnp.where(g == 0, r2, jnp.where(g == 1, r4, jnp.where(g == 2, r8, r16)))
        du_ref[...] = (win - dpooled).astype(du_ref.dtype)

    return _pcall(body, name=name, grid=(POOL_G, nb),
                  in_specs=[blk, blk, blk, pl.BlockSpec((1, LANE, LANE), lambda g, b: (g, 0, 0)),
                            pl.BlockSpec((1, LANE), lambda g, b: (0, g))],
                  out_specs=[blk, pl.BlockSpec((1, LANE, LANE), lambda g, b: (g, 0, 0)),
                             pl.BlockSpec((1, 1, LANE), lambda g, b: (b, 0, g))],
                  out_shape=[_sds((T, POOL_W), MX), _sds((POOL_G, LANE, LANE), F32), _sds((nb, 1, POOL_W), F32)],
                  )(dscaled, mixed, pooled, pool_grp, pool_scale)


def _lane_masks(shape):
    lane = lax.broadcasted_iota(jnp.int32, shape, len(shape) - 1)
    m_n = lane < NOPE
    m_r = jnp.logical_and(lane >= KR_LANE, lane < KR_LANE + ROPE)
    first_half = lane < KR_LANE + ROPE // 2
    return m_n, m_r, first_half


def _rot(y, first_half):
    return jnp.where(first_half, -pltpu.roll(y, LANE - ROPE // 2, 1), pltpu.roll(y, ROPE // 2, 1))


def _rot_t(v, first_half, m_r):
    return jnp.where(m_r, jnp.where(first_half, pltpu.roll(v, LANE - ROPE // 2, 1), -pltpu.roll(v, ROPE // 2, 1)), 0.0)


def _mla_in_fwd(z, w_q, w_kv, qa_gain, kva_gain, cos, sin, q_gain, k_gain, *, name, tm=256):
    T = z.shape[0]
    slab = pl.BlockSpec((tm, LANE), lambda i: (i, 0))
    vec = pl.BlockSpec((1, LANE), lambda i: (0, 0))
    whole = lambda a: pl.BlockSpec(a.shape, lambda i: (0, 0))

    def body(z_ref, wq_ref, wkv_ref, qa_ref, kva_ref, cos_ref, sin_ref, qg_ref, kg_ref,
             qn_ref, kvn_ref, qp_ref, kvp_ref, q_ref, k_ref, v_ref):
        ql = z_ref[:, Z_QL:Z_QL + QL].astype(F32)
        kvl = z_ref[:, Z_KV:Z_KV + KVL].astype(F32)
        qn = (ql * _rsq(ql) * qa_ref[...]).astype(MX)
        kvn = (kvl * _rsq(kvl) * kva_ref[...]).astype(MX)
        qn_ref[...] = qn
        kvn_ref[...] = kvn
        qp = jnp.dot(qn, wq_ref[...], preferred_element_type=F32)
        kvp = jnp.dot(kvn, wkv_ref[...], preferred_element_type=F32)
        qp_ref[...] = qp
        kvp_ref[...] = kvp
        m_n, m_r, first_half = _lane_masks((tm, LANE))
        c = cos_ref[...]
        s = sin_ref[...]
        qg = qg_ref[...]
        kg = kg_ref[...]
        xr = z_ref[:, Z_KR:Z_KR + LANE].astype(F32)
        rr = lax.rsqrt(jnp.sum(xr * xr, axis=-1, keepdims=True) * (1.0 / ROPE) + EPS)
        yr = xr * rr * kg
        kr = jnp.where(m_r, yr * c + _rot(yr, first_half) * s, 0.0)
        for h in range(NH):
            x = qp[:, h * LANE:(h + 1) * LANE]
            x2 = x * x
            rn = lax.rsqrt(jnp.sum(jnp.where(m_n, x2, 0.0), axis=-1, keepdims=True) * (1.0 / NOPE) + EPS)
            rq = lax.rsqrt(jnp.sum(jnp.where(m_r, x2, 0.0), axis=-1, keepdims=True) * (1.0 / ROPE) + EPS)
            y = x * jnp.where(m_n, rn, jnp.where(m_r, rq, 0.0)) * qg
            q_ref[:, h * LANE:(h + 1) * LANE] = ((y * c + _rot(y, first_half) * s) * ATTN_SCALE).astype(q_ref.dtype)
            xk = kvp[:, h * LANE:(h + 1) * LANE]
            rk = lax.rsqrt(jnp.sum(jnp.where(m_n, xk * xk, 0.0), axis=-1, keepdims=True) * (1.0 / NOPE) + EPS)
            k_ref[:, h * LANE:(h + 1) * LANE] = (jnp.where(m_n, xk * rk * kg, 0.0) + kr).astype(k_ref.dtype)
        v_ref[...] = kvp[:, NH * LANE:].astype(v_ref.dtype)

    return _pcall(body, name=name, grid=(T // tm,),
                  in_specs=[_row_spec(tm, LAT_W), whole(w_q), whole(w_kv), whole(qa_gain), whole(kva_gain), slab, slab, vec, vec],
                  out_specs=[_row_spec(tm, QL), _row_spec(tm, KVL), _row_spec(tm, NH * LANE), _row_spec(tm, NH * LANE + NH * NOPE),
                             _row_spec(tm, NH * LANE), _row_spec(tm, NH * LANE), _row_spec(tm, NH * NOPE)],
                  out_shape=[_sds((T, QL), MX), _sds((T, KVL), MX), _sds((T, NH * LANE), F32), _sds((T, NH * LANE + NH * NOPE), F32),
                             _sds((T, NH * LANE), MX), _sds((T, NH * LANE), MX), _sds((T, NH * NOPE), MX)],
                  )(z, w_q, w_kv, qa_gain, kva_gain, cos, sin, q_gain, k_gain)


def _mla_in_bwd(dq, dk, dv, qp, kvp, z, w_q, w_kv, qa_gain, kva_gain, cos, sin, q_gain, k_gain, *, nb, name, tm=256):
    T = dq.shape[0]
    tps = (T // nb) // tm
    slab = pl.BlockSpec((tm, LANE), lambda i: (i, 0))
    vec = pl.BlockSpec((1, LANE), lambda i: (0, 0))
    whole = lambda a: pl.BlockSpec(a.shape, lambda i: (0, 0))

    def latent_bwd(x, dy, gain):
        r = _rsq(x)
        xhat = x * r
        dxhat = dy * gain
        return r * (dxhat - xhat * jnp.mean(dxhat * xhat, axis=-1, keepdims=True)), jnp.sum(dy * xhat, axis=0, keepdims=True)

    def body(dq_ref, dk_ref, dv_ref, qp_ref, kvp_ref, z_ref, wq_ref, wkv_ref, qa_ref, kva_ref, cos_ref, sin_ref, qg_ref, kg_ref,
             dqp_ref, dkvp_ref, dkr_ref, dql_ref, dkvl_ref, st_ref, sq_ref, sk_ref):
        i = pl.program_id(0)
        qp, kvp = qp_ref, kvp_ref
        m_n, m_r, first_half = _lane_masks((tm, LANE))
        c = cos_ref[...]
        s = sin_ref[...]
        qg = qg_ref[...]
        kg = kg_ref[...]
        dqg = jnp.zeros((1, LANE), F32)
        dkg = jnp.zeros((1, LANE), F32)
        dkr_sum = jnp.zeros((tm, LANE), F32)
        for h in range(NH):
            x = qp[:, h * LANE:(h + 1) * LANE]
            x2 = x * x
            rn = lax.rsqrt(jnp.sum(jnp.where(m_n, x2, 0.0), axis=-1, keepdims=True) * (1.0 / NOPE) + EPS)
            rq = lax.rsqrt(jnp.sum(jnp.where(m_r, x2, 0.0), axis=-1, keepdims=True) * (1.0 / ROPE) + EPS)
            rfac = jnp.where(m_n, rn, jnp.where(m_r, rq, 0.0))
            xhat = x * rfac
            do = dq_ref[:, h * LANE:(h + 1) * LANE] * ATTN_SCALE
            dy = do * c + _rot_t(do * s, first_half, m_r)
            dqg = dqg + jnp.sum(dy * xhat, axis=0, keepdims=True)
            dxhat = dy * qg
            t = dxhat * xhat
            mean_n = jnp.sum(jnp.where(m_n, t, 0.0), axis=-1, keepdims=True) * (1.0 / NOPE)
            mean_r = jnp.sum(jnp.where(m_r, t, 0.0), axis=-1, keepdims=True) * (1.0 / ROPE)
            dqp_ref[:, h * LANE:(h + 1) * LANE] = (
                rfac * (dxhat - xhat * jnp.where(m_n, mean_n, jnp.where(m_r, mean_r, 0.0)))).astype(dqp_ref.dtype)

            xk = kvp[:, h * LANE:(h + 1) * LANE]
            rk = lax.rsqrt(jnp.sum(jnp.where(m_n, xk * xk, 0.0), axis=-1, keepdims=True) * (1.0 / NOPE) + EPS)
            khat = jnp.where(m_n, xk * rk, 0.0)
            dko = dk_ref[:, h * LANE:(h + 1) * LANE]
            dkn = jnp.where(m_n, dko, 0.0)
            dkg = dkg + jnp.sum(dkn * khat, axis=0, keepdims=True)
            dkhat = dkn * kg
            mean_k = jnp.sum(dkhat * khat, axis=-1, keepdims=True) * (1.0 / NOPE)
            dkvp_ref[:, h * LANE:(h + 1) * LANE] = jnp.where(m_n, rk * (dkhat - khat * mean_k), 0.0).astype(dkvp_ref.dtype)
            dkr_sum = dkr_sum + jnp.where(m_r, dko, 0.0)
        dkvp_ref[:, NH * LANE:] = dv_ref[...].astype(dkvp_ref.dtype)

        xr = z_ref[:, Z_KR:Z_KR + LANE].astype(F32)
        rr = lax.rsqrt(jnp.sum(xr * xr, axis=-1, keepdims=True) * (1.0 / ROPE) + EPS)
        rhat = xr * rr
        dyr = dkr_sum * c + _rot_t(dkr_sum * s, first_half, m_r)
        dkg = dkg + jnp.sum(dyr * rhat, axis=0, keepdims=True)
        drhat = dyr * kg
        mean_kr = jnp.sum(drhat * rhat, axis=-1, keepdims=True) * (1.0 / ROPE)
        dkr_ref[...] = jnp.where(m_r, rr * (drhat - rhat * mean_kr), 0.0).astype(dkr_ref.dtype)

        nt = (((1,), (1,)), ((), ()))
        dqn = lax.dot_general(dqp_ref[...], wq_ref[...], nt, preferred_element_type=F32)
        dkvn = lax.dot_general(dkvp_ref[...], wkv_ref[...], nt, preferred_element_type=F32)
        dql, dqa = latent_bwd(z_ref[:, Z_QL:Z_QL + QL].astype(F32), dqn, qa_ref[...])
        dkvl, dkva = latent_bwd(z_ref[:, Z_KV:Z_KV + KVL].astype(F32), dkvn, kva_ref[...])
        dql_ref[...] = dql.astype(dql_ref.dtype)
        dkvl_ref[...] = dkvl.astype(dkvl_ref.dtype)

        @pl.when(i % tps == 0)
        def _():
            st_ref[...] = jnp.zeros_like(st_ref)
            sq_ref[...] = jnp.zeros_like(sq_ref)
            sk_ref[...] = jnp.zeros_like(sk_ref)

        st_ref[0, 0:1, :] += dqg
        st_ref[0, 1:2, :] += dkg
        sq_ref[0, 0:1, :] += dqa
        sk_ref[0, 0:1, :] += dkva

    stat = lambda w: pl.BlockSpec((1, SUBLANE, w), lambda i: (i // tps, 0, 0))
    return _pcall(body, name=name, grid=(T // tm,),
                  in_specs=[_row_spec(tm, NH * LANE), _row_spec(tm, NH * LANE), _row_spec(tm, NH * NOPE),
                            _row_spec(tm, NH * LANE), _row_spec(tm, NH * LANE + NH * NOPE), _row_spec(tm, LAT_W),
                            whole(w_q), whole(w_kv), whole(qa_gain), whole(kva_gain), slab, slab, vec, vec],
                  out_specs=[_row_spec(tm, NH * LANE), _row_spec(tm, NH * LANE + NH * NOPE), slab, _row_spec(tm, QL),
                             _row_spec(tm, KVL), stat(LANE), stat(QL), stat(KVL)],
                  out_shape=[_sds((T, NH * LANE), MX), _sds((T, NH * LANE + NH * NOPE), MX), _sds((T, LANE), MX),
                             _sds((T, QL), MX), _sds((T, KVL), MX), _sds((nb, SUBLANE, LANE), F32),
                             _sds((nb, SUBLANE, QL), F32), _sds((nb, SUBLANE, KVL), F32)],
                  )(dq, dk, dv, qp, kvp, z, w_q, w_kv, qa_gain, kva_gain, cos, sin, q_gain, k_gain)


def _lower_triangle(t):
    return lax.broadcasted_iota(jnp.int32, (t, t), 1) <= lax.broadcasted_iota(jnp.int32, (t, t), 0)


def _attn_fwd(q, k, v, *, nb, name, tq=512, comm=None):
    T = q.shape[0]
    S = T // nb
    tq = _pick(S, tq)
    nq = S // tq
    tk = tq
    npair = NH // 2

    def body(q_ref, k_ref, v_ref, o_ref, lse_ref):
        qi = pl.program_id(2)
        lane = lax.broadcasted_iota(jnp.int32, (tq, LANE), 1)
        qs = [q_ref[:, hh * LANE:(hh + 1) * LANE] for hh in range(2)]

        def block(j, carry, diagonal):
            k0 = pl.multiple_of(j * tk, tk)
            vb = v_ref[pl.ds(k0, tk), :]
            vbs = [jnp.where(lane < NOPE, vb, jnp.ones_like(vb)), jnp.where(lane < NOPE, jnp.ones_like(vb), vb)]
            new = []
            for hh in range(2):
                m, acc = carry[hh]
                kb = k_ref[pl.ds(k0, tk), hh * LANE:(hh + 1) * LANE]
                s = lax.dot_general(qs[hh], kb, (((1,), (1,)), ((), ())), preferred_element_type=F32)
                if diagonal:
                    s = jnp.where(_lower_triangle(tq), s, NEG)
                m_new = jnp.maximum(m, jnp.max(s, axis=-1, keepdims=True))
                p = jnp.exp((s - m_new).astype(MX))
                acc = jnp.exp(m - m_new) * acc + jnp.dot(p, vbs[hh], preferred_element_type=F32)
                new.append((m_new, acc))
            return tuple(new)

        init = tuple((jnp.full((tq, 1), NEG, F32), jnp.zeros((tq, LANE), F32)) for _ in range(2))
        carry = lax.fori_loop(0, qi, lambda j, c: block(j, c, False), init)
        (m0, acc0), (m1, acc1) = block(qi, carry, True)
        l0, l1 = acc0[:, NOPE:NOPE + 1], acc1[:, 0:1]
        o_ref[...] = jnp.where(lane < NOPE, acc0 / l0, acc1 / l1).astype(o_ref.dtype)
        lse_ref[...] = jnp.where(lane < NOPE, m0 + jnp.log(l0), m1 + jnp.log(l1))

    return _pcall(body, name=name, grid=(nb, npair, nq),
                  in_specs=[pl.BlockSpec((tq, 2 * LANE), lambda b, p, i: (b * nq + i, p)),
                            pl.BlockSpec((S, 2 * LANE), lambda b, p, i: (b, p)),
                            pl.BlockSpec((S, LANE), lambda b, p, i: (b, p))],
                  out_specs=[pl.BlockSpec((tq, LANE), lambda b, p, i: (b * nq + i, p)),
                             pl.BlockSpec((tq, LANE), lambda b, p, i: (b * nq + i, p))],
                  out_shape=[_sds((T, NH * NOPE), MX), _sds((T, NH * NOPE), F32)], comm=comm)(q, k, v)


def _attn_bwd(q, k, v, o, lse, do, *, nb, name, tq=512, comm=None):
    T = q.shape[0]
    S = T // nb
    tq = _pick(S, tq)
    nq = S // tq
    tk = tq
    npair = NH // 2

    def body(q_ref, k_ref, v_ref, o_ref, lse_ref, do_ref, dq_ref, dk_ref, dv_ref, delta_ref):
        lane = lax.broadcasted_iota(jnp.int32, (tq, LANE), 1)
        first = lane < NOPE
        dq_ref[...] = jnp.zeros_like(dq_ref)

        def delta_step(qi, _):
            q0 = pl.multiple_of(qi * tq, tq)
            prod = do_ref[pl.ds(q0, tq), :] * o_ref[pl.ds(q0, tq), :].astype(F32)
            d0 = jnp.sum(jnp.where(first, prod, 0.0), axis=-1, keepdims=True)
            d1 = jnp.sum(jnp.where(first, 0.0, prod), axis=-1, keepdims=True)
            delta_ref[pl.ds(q0, tq), :] = jnp.where(first, d0, d1)
            return 0

        lax.fori_loop(0, nq, delta_step, 0)

        def kv_step(kj, _):
            k0 = pl.multiple_of(kj * tk, tk)
            kbs = [k_ref[pl.ds(k0, tk), hh * LANE:(hh + 1) * LANE] for hh in range(2)]
            vb = v_ref[pl.ds(k0, tk), :]

            def q_block(qi, carry, diagonal):
                dk0, dk1, dv = carry
                dks = [dk0, dk1]
                q0 = pl.multiple_of(qi * tq, tq)
                dov = do_ref[pl.ds(q0, tq), :]
                lse_v = lse_ref[pl.ds(q0, tq), :]
                delta_v = delta_ref[pl.ds(q0, tq), :]
                for hh in range(2):
                    qh = q_ref[pl.ds(q0, tq), hh * LANE:(hh + 1) * LANE]
                    dob = jnp.where(first if hh == 0 else jnp.logical_not(first), dov, 0.0).astype(MX)
                    s = lax.dot_general(qh, kbs[hh], (((1,), (1,)), ((), ())), preferred_element_type=F32)
                    p = jnp.exp(s - lse_v[:, hh * NOPE:hh * NOPE + 1])
                    if diagonal:
                        p = jnp.where(_lower_triangle(tq), p, 0.0)
                    dp = lax.dot_general(dob, vb, (((1,), (1,)), ((), ())), preferred_element_type=F32)
                    ds = (p * (dp - delta_v[:, hh * NOPE:hh * NOPE + 1])).astype(MX)
                    dks[hh] = dks[hh] + lax.dot_general(ds, qh, (((0,), (0,)), ((), ())), preferred_element_type=F32)
                    dv = dv + lax.dot_general(p.astype(MX), dob, (((0,), (0,)), ((), ())), preferred_element_type=F32)
                    dq_ref[pl.ds(q0, tq), hh * LANE:(hh + 1) * LANE] += jnp.dot(ds, kbs[hh], preferred_element_type=F32)
                return dks[0], dks[1], dv

            zero = jnp.zeros((tk, LANE), F32)
            carry = q_block(kj, (zero, zero, zero), True)
            dk0, dk1, dv = lax.fori_loop(kj + 1, nq, lambda qi, c: q_block(qi, c, False), carry)
            dk_ref[pl.ds(k0, tk), 0:LANE] = dk0
            dk_ref[pl.ds(k0, tk), LANE:2 * LANE] = dk1
            dv_ref[pl.ds(k0, tk), :] = dv
            return 0

        lax.fori_loop(0, nq, kv_step, 0)

    pair256 = pl.BlockSpec((S, 2 * LANE), lambda b, p: (b, p))
    pair128 = pl.BlockSpec((S, LANE), lambda b, p: (b, p))
    return _pcall(body, name=name, grid=(nb, npair),
                  in_specs=[pair256, pair256, pair128, pair128, pair128, pair128],
                  out_specs=[pair256, pair256, pair128],
                  out_shape=[_sds((T, NH * LANE), F32), _sds((T, NH * LANE), F32), _sds((T, NH * NOPE), F32)],
                  scratch=[pltpu.VMEM((S, LANE), F32)], comm=comm)(q, k, v, o, lse, do)


def _run(plan, fn, *args, name, **kw):
    rider = plan.rider(name)
    if rider is None:
        return fn(*args, name=name, **kw)
    outs, landed = fn(*args, name=name, comm=rider, **kw)
    plan.landed(name, landed)
    return outs


def _layer_fwd_bwd(x, tgt, mod3, cos, sin, plan, P):
    nb = mod3.shape[0]
    W = plan.W
    h1, gu1, a1 = _run(plan, _ffn_up_first, x, mod3, P["norm_ffn1"], W["ffn1_in"], sub=0, name="ffn1_up_a")
    gu1, a1 = _run(plan, _ffn_up_second, h1, W["ffn1_in"], gu1, a1, name="ffn1_up_b")
    f1, x1, h2 = _run(plan, _mm_resid_norm, a1, W["ffn1_out"], x, mod3, P["norm_mix"], sub=1, coef=0.5, name="ffn1_out")
    z = _run(plan, _mm, h2, W["w_in"], mode="nt", out_dtype=MX, name="mix_in", tn=1664)
    pooled, mixed, scaled = _pool_fwd(z, W["pool_grp"], P["pool_scale"], nb=nb, name="pool_fwd")
    br_pool = _mm(scaled, W["pool_proj"], mode="nn", out_dtype=MX, name="pool_proj")
    qn, kvn, qp, kvp, q, k, v = _mla_in_fwd(z, W["q_up"], W["kv_up"], P["q_a_norm"], P["kv_a_norm"], cos, sin,
                                            P["q_gain"], P["k_gain"], name="mla_in_fwd")
    attn, lse = _run(plan, _attn_fwd, q, k, v, nb=nb, name="attn_fwd")
    br_mla, merged, mo, x2, h3 = _mix_out_fwd(z, br_pool, attn, W["mla_proj"], W["w_out"], x1, mod3, P["norm_ffn2"], sub=2,
                                              name="mix_out", tm=256)
    gu2, a2 = _ffn_up(h3, W["ffn2_in"], name="ffn2_up")
    dy, df2, st_fin, loss = _mm_loss(a2, W["ffn2_out"], x2, tgt, mod3, name="ffn2_out_loss")

    plan.grad("ffn2_out", _dw(a2, df2, name="d_ffn2_out"))
    dgu2 = _ffn_dact(df2, gu2, W["ffn2_out"], name="ffn2_dact")
    plan.grad("ffn2_in", _ffn_dw_in(h3, dgu2, name="d_ffn2_in"))
    dx2, dmo, st3 = _run(plan, _ffn_dh_norm, dgu2, W["ffn2_in"], x2, dy, mo, mod3, P["norm_ffn2"], sub=2, coef=1.0,
                         name="d_ffn2_h")
    plan.grad("w_out", _dw(merged, dmo, name="d_mix_out"))
    dbr_pool, dbr_mla, dgates, dattn = _mix_out_bwd(dmo, W["w_out"], z, br_pool, br_mla, W["mla_proj"], name="mix_out_bwd", tm=256)
    plan.grad("pool_proj", _dw(scaled, dbr_pool, name="d_pool_proj"))
    dscaled = _mm(dbr_pool, W["pool_proj"], mode="nt", out_dtype=F32, name="d_pool_scaled")
    du_pool, d_pool_grp, d_pool_scale = _pool_bwd(dscaled, mixed, pooled, W["pool_grp"], P["pool_scale"], nb=nb, name="pool_bwd")
    plan.grad("mla_proj", _dw(attn, dbr_mla, name="d_mla_proj"))
    dq, dk, dv = _run(plan, _attn_bwd, q, k, v, attn, lse, dattn, nb=nb, name="attn_bwd")
    dqp, dkvp, dkr, dql, dkvl, st_prep, st_q, st_kv = _mla_in_bwd(
        dq, dk, dv, qp, kvp, z, W["q_up"], W["kv_up"], P["q_a_norm"], P["kv_a_norm"], cos, sin, P["q_gain"], P["k_gain"],
        nb=nb, name="mla_in_bwd")
    plan.grad("q_up", _dw(qn, dqp, name="d_q_up"))
    plan.grad("kv_up", _dw(kvn, dkvp, name="d_kv_up"))
    dz = jnp.concatenate([du_pool, dql, dkvl, dkr, dgates], axis=1)
    plan.grad("w_in", _run(plan, _dw, h2, dz, name="d_mix_in", tm=256, tn=1664, out_t=True))
    dx1, df1, st2 = _run(plan, _mm_norm_bwd, dz, W["w_in"], x1, dx2, f1, mod3, P["norm_mix"], sub=1, coef=0.5, name="d_mix_h")
    plan.grad("ffn1_out", _run(plan, _dw, a1, df1, name="d_ffn1_out"))
    dgu1 = _run(plan, _ffn_dact, df1, gu1, W["ffn1_out"], name="ffn1_dact")
    half = x.shape[1] // 2
    plan.grad("ffn1_in@0", _run(plan, _ffn_dw_in, h1, dgu1, rows=(0, half), name="d_ffn1_in_a"))
    plan.grad("ffn1_in@1", _run(plan, _ffn_dw_in, h1, dgu1, rows=(half, half), name="d_ffn1_in_b"))
    dh1 = _run(plan, _ffn_dh, dgu1, W["ffn1_in"], name="d_ffn1_h")
    grad_x, st1 = _run(plan, _bwd_block, x, dh1, dx1, None, mod3, P["norm_ffn1"], sub=0, coef=0.0, name="bwd_norm1")

    return loss, grad_x, (st1, st2, st3, st_fin, st_q, st_kv, st_prep, d_pool_scale), d_pool_grp


def _padded(rows):
    return rows + -rows % (4 * SUBLANE)


def _pad_rows(a):
    pad = [(0, 0)] * a.ndim
    pad[-2] = (0, _padded(a.shape[-2]) - a.shape[-2])
    return jnp.pad(a, pad)


def _w_in_pieces(shard):
    runs = []
    for ref0, ref1, k0 in ((0, 1152, 0), (1152, 1184, Z_KR + KR_LANE), (1184, N_CHIP * shard, Z_GP)):
        for q in range(N_CHIP):
            lo, hi = max(ref0, q * shard), min(ref1, (q + 1) * shard)
            if lo < hi:
                runs.append((k0 + lo - ref0, q * _padded(shard) + lo - q * shard, hi - lo))
    return runs


def _w_in_stacked_to_kernel(st, shard):
    flat = st.reshape(-1, st.shape[2])
    parts, at = [], 0
    for k0, s0, n in sorted(_w_in_pieces(shard)):
        parts += [jnp.zeros((k0 - at, flat.shape[1]), st.dtype)] * (k0 > at) + [flat[s0:s0 + n]]
        at = k0 + n
    assert at == Z_W
    return jnp.concatenate(parts, axis=0)


def _w_in_kernel_to_stacked(gt, shard):
    parts, at = [], 0
    for s0, k0, n in sorted((s, k, n) for k, s, n in _w_in_pieces(shard)) + [(N_CHIP * _padded(shard), 0, 0)]:
        parts += [jnp.zeros((s0 - at, gt.shape[1]), gt.dtype)] * (s0 > at) + [gt[k0:k0 + n]] * (n > 0)
        at = s0 + n
    return jnp.concatenate(parts, axis=0).reshape(N_CHIP, _padded(shard), gt.shape[1])


def _q_up_to_kernel(w):
    k = w.shape[0]
    return jnp.pad(w.reshape(k, NH, NOPE + ROPE), ((0, 0), (0, 0), (0, LANE - NOPE - ROPE))).reshape(k, NH * LANE)


def _q_up_from_kernel(g):
    k = g.shape[0]
    return g.reshape(k, NH, LANE)[:, :, :NOPE + ROPE].reshape(k, NH * (NOPE + ROPE))


def _kv_up_to_kernel(w):
    k = w.shape[0]
    w3 = w.reshape(k, NH, 2 * NOPE)
    kpart = jnp.pad(w3[:, :, :NOPE], ((0, 0), (0, 0), (0, LANE - NOPE))).reshape(k, NH * LANE)
    return jnp.concatenate([kpart, w3[:, :, NOPE:].reshape(k, NH * NOPE)], axis=1)


def _kv_up_from_kernel(g):
    k = g.shape[0]
    kpart = g[:, :NH * LANE].reshape(k, NH, LANE)[:, :, :NOPE]
    vpart = g[:, NH * LANE:].reshape(k, NH, NOPE)
    return jnp.concatenate([kpart, vpart], axis=2).reshape(k, NH * 2 * NOPE)


def _gain_slab(nope, rope):
    return jnp.concatenate([nope, rope, jnp.zeros((1, LANE - NOPE - ROPE), nope.dtype)], axis=1)


def _rope_tables(positions):
    inv_freq = 10000.0 ** (-jnp.arange(0, ROPE, 2, dtype=F32) / ROPE)
    ang = positions.astype(F32)[:, None] * inv_freq
    ang = jnp.concatenate([ang, ang], axis=-1)
    t = positions.shape[0]
    cos = jnp.concatenate([jnp.ones((t, KR_LANE), F32), jnp.cos(ang), jnp.ones((t, LANE - KR_LANE - ROPE), F32)], axis=1)
    sin = jnp.concatenate([jnp.zeros((t, KR_LANE), F32), jnp.sin(ang), jnp.zeros((t, LANE - KR_LANE - ROPE), F32)], axis=1)
    return cos, sin


def _coords():
    return lax.axis_index("x"), lax.axis_index("y"), lax.axis_index("c")


HBM_SPEC = pl.BlockSpec(memory_space=pl.ANY)
VMEM_SPEC = pl.BlockSpec(memory_space=pltpu.VMEM)


CHIP_RELS = ((1, 0), (0, 1), (1, 1))


def _gather_comm(shards):
    n = len(shards)

    def ici(ins, outs, sems, a, j, x, y, cc):
        half = ins[a].shape[0] // 2
        mine = pl.ds(cc * half, half)
        dx, dy = CHIP_RELS[j]
        return pltpu.make_async_remote_copy(src_ref=ins[a].at[mine], dst_ref=outs[a].at[2 * x + y, mine],
                                            send_sem=sems[0].at[a, j], recv_sem=sems[1].at[a, j],
                                            device_id=(x ^ dx, y ^ dy, cc), device_id_type=MESH)

    def d2d(ins, outs, sems, a, j, x, y, cc, half_of):
        half = ins[a].shape[0] // 2
        dx, dy = CHIP_RELS[j]
        landed = outs[a].at[2 * (x ^ dx) + (y ^ dy), pl.ds(half_of * half, half)]
        return pltpu.make_async_remote_copy(src_ref=landed, dst_ref=landed, send_sem=sems[2].at[a, j], recv_sem=sems[3].at[a, j],
                                            device_id=(x, y, 1 - cc), device_id_type=MESH)

    def own(ins, outs, sems, a, x, y):
        return pltpu.make_async_copy(ins[a], outs[a].at[2 * x + y], sems[4].at[a])

    def start(ins, outs, sems):
        x, y, cc = _coords()
        for a in range(n):
            own(ins, outs, sems, a, x, y).start()
            for j in range(3):
                ici(ins, outs, sems, a, j, x, y, cc).start()

    def finish(ins, outs, sems):
        x, y, cc = _coords()
        for a in range(n):
            for j in range(3):
                ici(ins, outs, sems, a, j, x, y, cc).wait_recv()
                d2d(ins, outs, sems, a, j, x, y, cc, cc).start()
        for a in range(n):
            for j in range(3):
                d2d(ins, outs, sems, a, j, x, y, cc, 1 - cc).wait_recv()
        for a in range(n):
            for j in range(3):
                ici(ins, outs, sems, a, j, x, y, cc).wait_send()
                d2d(ins, outs, sems, a, j, x, y, cc, cc).wait_send()
            own(ins, outs, sems, a, x, y).wait()

    dma = pltpu.SemaphoreType.DMA
    return _Comm(shards, [_sds((N_CHIP,) + s.shape, s.dtype) for s in shards],
                 [dma((n, 3)), dma((n, 3)), dma((n, 3)), dma((n, 3)), dma((n,))], start, finish)


def _swap_comm(parts):
    n = len(parts)

    def copy(ins, outs, sems, a):
        x, y, cc = _coords()
        half = ins[a].shape[1] // 2
        return pltpu.make_async_remote_copy(src_ref=ins[a].at[:, pl.ds((1 - cc) * half, half)], dst_ref=outs[a],
                                            send_sem=sems[0].at[a], recv_sem=sems[1].at[a], device_id=(x, y, 1 - cc),
                                            device_id_type=MESH)

    def start(ins, outs, sems):
        for a in range(n):
            copy(ins, outs, sems, a).start()

    def finish(ins, outs, sems):
        for a in range(n):
            copy(ins, outs, sems, a).wait()

    dma = pltpu.SemaphoreType.DMA
    return _Comm(parts, [_sds((p.shape[0], p.shape[1] // 2, p.shape[2]), p.dtype) for p in parts], [dma((n,)), dma((n,))],
                 start, finish)


def _add_half(full, other, cidx, *, name):
    nch, r, c = full.shape
    half = r // 2
    tr = _pick_rows(half)
    nbk = half // tr
    grid_spec = pltpu.PrefetchScalarGridSpec(
        num_scalar_prefetch=1, grid=(nch, nbk),
        in_specs=[pl.BlockSpec((1, tr, c), lambda j, i, cref: (j, cref[0] * nbk + i, 0)),
                  pl.BlockSpec((1, tr, c), lambda j, i, cref: (j, i, 0))],
        out_specs=pl.BlockSpec((1, tr, c), lambda j, i, cref: (j, i, 0)))

    def body(cref, a_ref, b_ref, o_ref):
        o_ref[...] = (a_ref[...] + b_ref[...]).astype(o_ref.dtype)

    return _pcall(body, name=name, out_shape=_sds((nch, half, c), MX), grid_spec=grid_spec)(cidx, full, other)


def _pick_rows(rows, target=512):
    best = None
    for t in range(16, min(rows, target) + 1, 16):
        if rows % t == 0:
            best = t
    return rows if best is None else best


def _exchange_comm(parts):
    n = len(parts)

    def send(ins, outs, sems, a, j, x, y, cc):
        dx, dy = CHIP_RELS[j]
        return pltpu.make_async_remote_copy(src_ref=ins[a].at[2 * (x ^ dx) + (y ^ dy)], dst_ref=outs[a].at[2 * x + y],
                                            send_sem=sems[0].at[a, j], recv_sem=sems[1].at[a, j],
                                            device_id=(x ^ dx, y ^ dy, cc), device_id_type=MESH)

    def landing(ins, outs, sems, a, j, x, y, cc):
        dx, dy = CHIP_RELS[j]
        peer_chip = 2 * (x ^ dx) + (y ^ dy)
        return pltpu.make_async_remote_copy(src_ref=ins[a].at[peer_chip], dst_ref=outs[a].at[peer_chip], send_sem=sems[0].at[a, j],
                                            recv_sem=sems[1].at[a, j], device_id=(x, y, cc), device_id_type=MESH)

    def own(ins, outs, sems, a, x, y):
        return pltpu.make_async_copy(ins[a].at[2 * x + y], outs[a].at[2 * x + y], sems[2].at[a])

    def start(ins, outs, sems):
        x, y, cc = _coords()
        for a in range(n):
            own(ins, outs, sems, a, x, y).start()
            for j in range(3):
                send(ins, outs, sems, a, j, x, y, cc).start()

    def finish(ins, outs, sems):
        x, y, cc = _coords()
        for a in range(n):
            for j in range(3):
                landing(ins, outs, sems, a, j, x, y, cc).wait_recv()
        for a in range(n):
            for j in range(3):
                send(ins, outs, sems, a, j, x, y, cc).wait_send()
            own(ins, outs, sems, a, x, y).wait()

    dma = pltpu.SemaphoreType.DMA
    return _Comm(parts, [_sds(p.shape, p.dtype) for p in parts], [dma((n, 3)), dma((n, 3)), dma((n,))], start, finish)


def _sum_chips(q, cidx, *, name):
    nch, h, c = q.shape
    tr = _pick_rows(h)
    nbk = h // tr
    grid_spec = pltpu.PrefetchScalarGridSpec(
        num_scalar_prefetch=1, grid=(nbk,),
        in_specs=[pl.BlockSpec((nch, tr, c), lambda i, cref: (0, i, 0))],
        out_specs=pl.BlockSpec((tr, c), lambda i, cref: (cref[0] * nbk + i, 0)))

    def body(cref, q_ref, o_ref):
        acc = q_ref[0].astype(F32) + q_ref[1].astype(F32)
        acc = acc + q_ref[2].astype(F32)
        o_ref[...] = acc + q_ref[3].astype(F32)

    return _pcall(body, name=name, out_shape=_sds((2 * h, c), F32), grid_spec=grid_spec)(cidx, q)


def _join_comm(fulls):
    n = len(fulls)

    def half_copy(outs, sems, a, which):
        x, y, cc = _coords()
        h = outs[a].shape[0] // 2
        rows = outs[a].at[pl.ds((cc if which == 0 else 1 - cc) * h, h)]
        return pltpu.make_async_remote_copy(src_ref=rows, dst_ref=rows, send_sem=sems[0].at[a], recv_sem=sems[1].at[a],
                                            device_id=(x, y, 1 - cc), device_id_type=MESH)

    def start(ins, outs, sems):
        for a in range(n):
            half_copy(outs, sems, a, 0).start()

    def finish(ins, outs, sems):
        for a in range(n):
            half_copy(outs, sems, a, 1).wait_recv()
        for a in range(n):
            half_copy(outs, sems, a, 0).wait_send()

    dma = pltpu.SemaphoreType.DMA
    return _Comm(fulls, [_sds(p.shape, p.dtype) for p in fulls], [dma((n,)), dma((n,))], start, finish,
                 aliases={a: a for a in range(n)})


def _ada_prologue(c, w, b, *, name, comm=None):
    nb, dm = c.shape
    n = w.shape[1]

    def body(c_ref, w_ref, b_ref, call_ref, land_ref, cpad, mod_s, g_send, g_recv, m_send, m_recv):
        x, y, cc = _coords()
        me = 4 * x + 2 * y + cc
        chip = 2 * x + y
        cpad[...] = jnp.zeros_like(cpad)
        cpad[0:nb, :] = c_ref[...]
        copies = []
        for kk in range(1, N_DEV):
            peer = (x ^ (kk >> 2), y ^ ((kk >> 1) & 1), cc ^ (kk & 1))
            cp = pltpu.make_async_remote_copy(src_ref=cpad, dst_ref=call_ref.at[me], send_sem=g_send.at[kk - 1],
                                              recv_sem=g_recv.at[kk - 1], device_id=peer, device_id_type=MESH)
            cp.start()
            copies.append(cp)
        call_ref[me] = cpad[...]
        for kk in range(1, N_DEV):
            pltpu.make_async_remote_copy(src_ref=cpad, dst_ref=call_ref.at[me ^ kk], send_sem=g_send.at[kk - 1],
                                         recv_sem=g_recv.at[kk - 1], device_id=(x, y, cc), device_id_type=MESH).wait_recv()
        cv = call_ref[...].reshape(N_DEV * SUBLANE, dm)
        act = (cv * _sigmoid(cv)).astype(MX)
        mod = jnp.dot(act, w_ref[...].astype(MX), preferred_element_type=F32) + b_ref[...]
        mod_s[...] = mod.reshape(N_DEV, SUBLANE, n)
        for j, (dx, dy) in enumerate(CHIP_RELS):
            cp = pltpu.make_async_remote_copy(src_ref=mod_s.at[4 * (x ^ dx) + 2 * (y ^ dy) + cc], dst_ref=land_ref.at[chip],
                                              send_sem=m_send.at[j], recv_sem=m_recv.at[j],
                                              device_id=(x ^ dx, y ^ dy, cc), device_id_type=MESH)
            cp.start()
            copies.append(cp)
        land_ref[chip] = mod_s[me]
        for j, (dx, dy) in enumerate(CHIP_RELS):
            pltpu.make_async_remote_copy(src_ref=mod_s.at[me], dst_ref=land_ref.at[2 * (x ^ dx) + (y ^ dy)], send_sem=m_send.at[j],
                                         recv_sem=m_recv.at[j], device_id=(x, y, cc), device_id_type=MESH).wait_recv()
        for cp in copies:
            cp.wait_send()

    dma = pltpu.SemaphoreType.DMA
    return _pcall(body, name=name, in_specs=[VMEM_SPEC] * 3, out_specs=[VMEM_SPEC] * 2,
                  out_shape=[_sds((N_DEV, SUBLANE, dm), F32), _sds((N_CHIP, SUBLANE, n), F32)],
                  scratch=[pltpu.VMEM((SUBLANE, dm), F32), pltpu.VMEM((N_DEV, SUBLANE, n), F32),
                           dma((N_DEV - 1,)), dma((N_DEV - 1,)), dma((3,)), dma((3,))], comm=comm)(c, w, b)


def _ada_bwd(c_all, dmod_cols, *, name):
    m, kdim = c_all.shape
    n = dmod_cols.shape[1]
    tn = _pick(n, 1152)

    def body(c_ref, d_ref, o_ref):
        cv = c_ref[...]
        act = (cv * _sigmoid(cv)).astype(MX)
        o_ref[...] = lax.dot_general(act, d_ref[...].astype(MX), (((0,), (0,)), ((), ())), preferred_element_type=F32)

    return _pcall(body, name=name, out_shape=_sds((kdim, n), F32), grid=(n // tn,),
                  in_specs=[pl.BlockSpec((m, kdim), lambda j: (0, 0)), pl.BlockSpec((m, tn), lambda j: (0, j))],
                  out_specs=pl.BlockSpec((kdim, tn), lambda j: (0, j)))(c_all, dmod_cols)


SLAB_W = 1024
RED_ROWS = 8


LOSS_LANE = SLAB_W - LANE


def _pack_stats(st1, st2, st3, st_fin, st_q, st_kv, st_prep, d_pool_scale, loss8, *, name):
    nb = st1.shape[0]
    dm_rows = -(-9 * nb // SUBLANE) * SUBLANE

    def body(s1, s2, s3, sf, sq, skv, sp, sps, loss_ref, o_ref):
        o_ref[...] = jnp.zeros_like(o_ref)
        for s in range(nb):
            rows = [s1[s, 0:1, :], s1[s, 1:2, :], s2[s, 3:4, :], s2[s, 0:1, :], s2[s, 1:2, :], s3[s, 3:4, :], s3[s, 0:1, :],
                    s3[s, 1:2, :], sf[s, 0:1, :]]
            for k, row in enumerate(rows):
                o_ref[9 * s + k:9 * s + k + 1, :] = row

        def over_seq(ref, r):
            acc = ref[0, r:r + 1, :]
            for s in range(1, nb):
                acc = acc + ref[s, r:r + 1, :]
            return acc

        o_ref[dm_rows + 0:dm_rows + 1, :] = over_seq(s1, 2)
        o_ref[dm_rows + 1:dm_rows + 2, :] = over_seq(s2, 2)
        o_ref[dm_rows + 2:dm_rows + 3, :] = over_seq(s3, 2)
        o_ref[dm_rows + 3:dm_rows + 4, 0:POOL_W] = over_seq(sps, 0)
        o_ref[dm_rows + 4:dm_rows + 5, 0:QL] = over_seq(sq, 0)
        o_ref[dm_rows + 5:dm_rows + 6, 0:KVL] = over_seq(skv, 0)
        o_ref[dm_rows + 6:dm_rows + 7, 0:LANE] = over_seq(sp, 0)
        o_ref[dm_rows + 7:dm_rows + 8, 0:LANE] = over_seq(sp, 1)
        o_ref[dm_rows + 7:dm_rows + 8, LOSS_LANE:] = loss_ref[0:1, :]

    return _pcall(body, name=name, out_shape=_sds((dm_rows + RED_ROWS, SLAB_W), F32), in_specs=[VMEM_SPEC] * 9,
                  out_specs=VMEM_SPEC)(st1, st2, st3, st_fin, st_q, st_kv, st_prep, d_pool_scale, loss8)


def _small_allreduce(slab, pool, *, name, comm=None):
    rows, w = slab.shape
    dm = rows - RED_ROWS
    prow, pw = pool.shape
    crow = RED_ROWS + 2 * dm

    def body(slab_ref, pool_ref, red_ref, dmod_ref, ptot_ref, sib_slab, sib_pool, chip_slab, chip_pool, land_slab, land_pool,
             a_send, a_recv, b_send, b_recv):
        x, y, cc = _coords()
        chip = 2 * x + y
        sib = (x, y, 1 - cc)
        to_sib = [pltpu.make_async_remote_copy(src_ref=slab_ref, dst_ref=sib_slab, send_sem=a_send.at[0], recv_sem=a_recv.at[0],
                                               device_id=sib, device_id_type=MESH),
                  pltpu.make_async_remote_copy(src_ref=pool_ref, dst_ref=sib_pool, send_sem=a_send.at[1], recv_sem=a_recv.at[1],
                                               device_id=sib, device_id_type=MESH)]
        for cp in to_sib:
            cp.start()
        for cp in to_sib:
            cp.wait()
        mine_dm, theirs_dm = slab_ref[0:dm, :], sib_slab[0:dm, :]
        chip_slab[0:RED_ROWS, :] = slab_ref[dm:, :] + sib_slab[dm:, :]
        chip_slab[RED_ROWS:RED_ROWS + dm, :] = jnp.where(cc == 0, mine_dm, theirs_dm)
        chip_slab[RED_ROWS + dm:, :] = jnp.where(cc == 0, theirs_dm, mine_dm)
        chip_pool[...] = pool_ref[...] + sib_pool[...]

        sends = []
        for j, (dx, dy) in enumerate(CHIP_RELS):
            peer = (x ^ dx, y ^ dy, cc)
            sends.append(pltpu.make_async_remote_copy(src_ref=chip_slab, dst_ref=land_slab.at[chip], send_sem=b_send.at[j, 0],
                                                      recv_sem=b_recv.at[j, 0], device_id=peer, device_id_type=MESH))
            sends.append(pltpu.make_async_remote_copy(src_ref=chip_pool, dst_ref=land_pool.at[chip], send_sem=b_send.at[j, 1],
                                                      recv_sem=b_recv.at[j, 1], device_id=peer, device_id_type=MESH))
        for cp in sends:
            cp.start()
        land_slab[chip] = chip_slab[...]
        land_pool[chip] = chip_pool[...]
        for j, (dx, dy) in enumerate(CHIP_RELS):
            peer_chip = 2 * (x ^ dx) + (y ^ dy)
            pltpu.make_async_remote_copy(src_ref=chip_slab, dst_ref=land_slab.at[peer_chip], send_sem=b_send.at[j, 0],
                                         recv_sem=b_recv.at[j, 0], device_id=(x, y, cc), device_id_type=MESH).wait_recv()
            pltpu.make_async_remote_copy(src_ref=chip_pool, dst_ref=land_pool.at[peer_chip], send_sem=b_send.at[j, 1],
                                         recv_sem=b_recv.at[j, 1], device_id=(x, y, cc), device_id_type=MESH).wait_recv()
        red = land_slab[0, 0:RED_ROWS, :]
        ptot = land_pool[0]
        for ch in range(1, N_CHIP):
            red = red + land_slab[ch, 0:RED_ROWS, :]
            ptot = ptot + land_pool[ch]
        red_ref[...] = red
        ptot_ref[...] = ptot
        for ch in range(N_CHIP):
            dmod_ref[2 * dm * ch:2 * dm * (ch + 1), :] = land_slab[ch, RED_ROWS:, :]
        for cp in sends:
            cp.wait_send()

    dma = pltpu.SemaphoreType.DMA
    return _pcall(body, name=name, in_specs=[VMEM_SPEC, VMEM_SPEC], out_specs=[VMEM_SPEC] * 3,
                  out_shape=[_sds((RED_ROWS, w), F32), _sds((N_DEV * dm, w), F32), _sds((prow, pw), F32)],
                  scratch=[pltpu.VMEM((rows, w), F32), pltpu.VMEM((prow, pw), F32), pltpu.VMEM((crow, w), F32),
                           pltpu.VMEM((prow, pw), F32), pltpu.VMEM((N_CHIP, crow, w), F32), pltpu.VMEM((N_CHIP, prow, pw), F32),
                           dma((2,)), dma((2,)), dma((3, 2)), dma((3, 2))], comm=comm)(slab, pool)


def _adamw_math(w, g, m, v):
    mn = ADAM_B1 * m + (1.0 - ADAM_B1) * g
    vn = ADAM_B2 * v + (1.0 - ADAM_B2) * (g * g)
    bc1 = 1.0 / (1.0 - ADAM_B1 ** ADAM_STEP)
    bc2 = 1.0 / (1.0 - ADAM_B2 ** ADAM_STEP)
    return -ADAM_LR * ((mn * bc1) / (jnp.sqrt(vn * bc2) + ADAM_EPS) + ADAM_WD * w), mn, vn


def _small_update(red, dmod_all, pool_total, nb, params, *, name):
    names = list(SMALL)
    dm = dmod_all.shape[0] // N_DEV

    def grad_of(nm, red_ref, dmod_ref, ptot_ref):
        if nm == "b_ada":
            acc = None
            for d in range(N_DEV):
                for s in range(nb):
                    blk = dmod_ref[d * dm + 9 * s:d * dm + 9 * s + 9, :]
                    acc = blk if acc is None else acc + blk
            return jnp.concatenate([acc[k:k + 1, :] for k in range(9)], axis=1)
        if nm == "pool_grp":
            return ptot_ref[...]
        row, lo, n = {"norm_ffn1": (0, 0, D), "norm_mix": (1, 0, D), "norm_ffn2": (2, 0, D), "pool_scale": (3, 0, POOL_W),
                      "q_a_norm": (4, 0, QL), "kv_a_norm": (5, 0, KVL), "q_norm_nope": (6, 0, NOPE),
                      "q_norm_rope": (6, NOPE, ROPE), "k_norm_nope": (7, 0, NOPE), "k_norm_rope": (7, KR_LANE, ROPE)}[nm]
        return red_ref[row:row + 1, lo:lo + n]

    def body(*refs):
        red_ref, dmod_ref, ptot_ref = refs[:3]
        ins = refs[3:3 + 3 * len(names)]
        outs = refs[3 + 3 * len(names):]
        outs[4 * len(names)][...] = red_ref[RED_ROWS - 1:RED_ROWS, LOSS_LANE:]
        for i, nm in enumerate(names):
            g = grad_of(nm, red_ref, dmod_ref, ptot_ref)
            d, mn, vn = _adamw_math(ins[3 * i][...], g, ins[3 * i + 1][...], ins[3 * i + 2][...])
            outs[4 * i][...] = g
            outs[4 * i + 1][...] = d
            outs[4 * i + 2][...] = mn
            outs[4 * i + 3][...] = vn

    flat_in = [a for nm in names for a in params[nm]]
    out_shape = [_sds(params[nm][0].shape, F32) for nm in names for _ in range(4)] + [_sds((1, LANE), F32)]
    res = _pcall(body, name=name, in_specs=[VMEM_SPEC] * (3 + len(flat_in)), out_specs=[VMEM_SPEC] * len(out_shape),
                 out_shape=out_shape)(red, dmod_all, pool_total, *flat_in)
    return {nm: tuple(res[4 * i:4 * i + 4]) for i, nm in enumerate(names)}, res[-1]


def _adamw(w, g, m, v, *, name, comm=None):
    r, c = w.shape
    tr = _pick_rows(r, 256)
    tc = c if tr < r else _pick(c, 256)
    spec = pl.BlockSpec((tr, tc), lambda i, j: (i, j))
    bc1 = 1.0 / (1.0 - ADAM_B1 ** ADAM_STEP)
    bc2 = 1.0 / (1.0 - ADAM_B2 ** ADAM_STEP)

    def body(w_ref, g_ref, m_ref, v_ref, d_ref, mo_ref, vo_ref):
        gv = g_ref[...]
        mn = ADAM_B1 * m_ref[...] + (1.0 - ADAM_B1) * gv
        vn = ADAM_B2 * v_ref[...] + (1.0 - ADAM_B2) * (gv * gv)
        mo_ref[...] = mn
        vo_ref[...] = vn
        d_ref[...] = -ADAM_LR * ((mn * bc1) / (jnp.sqrt(vn * bc2) + ADAM_EPS) + ADAM_WD * w_ref[...])

    out = _sds((r, c), F32)
    return _pcall(body, name=name, out_shape=[out, out, out], grid=(r // tr, c // tc), in_specs=[spec] * 4, out_specs=[spec] * 3,
                  comm=comm)(w, g, m, v)


BIG = ("w_ffn1_in", "w_ffn1_out", "w_in", "w_pool_proj", "w_q_up", "w_kv_up", "w_mla_proj", "w_out", "w_ffn2_in", "w_ffn2_out")
ROW_SHARDED = ("w_ffn1_out", "w_out", "w_ffn2_out")
KERNEL_NAME = {"w_ffn1_in": "ffn1_in", "w_ffn1_out": "ffn1_out", "w_in": "w_in", "w_pool_proj": "pool_proj", "w_q_up": "q_up",
               "w_kv_up": "kv_up", "w_mla_proj": "mla_proj", "w_out": "w_out", "w_ffn2_in": "ffn2_in", "w_ffn2_out": "ffn2_out"}
WEIGHTS = ("w_ada", "b_ada", "norm_ffn1", "w_ffn1_in", "w_ffn1_out", "norm_mix", "w_in", "pool_grp", "pool_scale", "w_pool_proj",
           "q_a_norm", "w_q_up", "kv_a_norm", "w_kv_up", "q_norm_nope", "q_norm_rope", "k_norm_nope", "k_norm_rope", "w_mla_proj",
           "w_out", "norm_ffn2", "w_ffn2_in", "w_ffn2_out")
SMALL = ("b_ada", "norm_ffn1", "norm_mix", "pool_grp", "pool_scale", "q_a_norm", "kv_a_norm", "q_norm_nope", "q_norm_rope",
         "k_norm_nope", "k_norm_rope", "norm_ffn2")


def _assemble(name, stacked):
    if name in ROW_SHARDED:
        return stacked.reshape(stacked.shape[0] * stacked.shape[1], stacked.shape[2])
    return jnp.transpose(stacked, (1, 0, 2)).reshape(stacked.shape[1], stacked.shape[0] * stacked.shape[2])


def _split(name, full):
    if name in ROW_SHARDED:
        return full.reshape(N_CHIP, full.shape[0] // N_CHIP, full.shape[1])
    return jnp.transpose(full.reshape(full.shape[0], N_CHIP, full.shape[1] // N_CHIP), (1, 0, 2))


GU = ("w_ffn1_in", "w_ffn2_in")
NARROW = ("w_in", "w_q_up")
TRANSPOSED = ("w_in",)
W_IN_SHARD = (Z_W - LANE + ROPE) // N_CHIP


MIX_SMALL = ("w_out", "w_pool_proj", "w_mla_proj", "w_q_up", "w_kv_up")
RIDES = {
    "ada_prologue": (("gather", ("w_ffn1_in",)),),
    "ffn1_up_a": (("gather", ("w_ffn1_out",)),),
    "ffn1_up_b": (("gather", ("w_in",)),),
    "mix_in": (("gather", MIX_SMALL),),
    "attn_fwd": (("gather", ("w_ffn2_in", "w_ffn2_out")),),
    "d_ffn2_h": (("swap", ("w_ffn2_out", "w_ffn2_in")),),
    "attn_bwd": (("exchange", ("w_ffn2_out", "w_ffn2_in")),),
    "d_mix_in": (("swap", MIX_SMALL),),
    "d_mix_h": (("exchange", MIX_SMALL), ("swap", ("w_in",))),
    "ffn1_dact": (("exchange", ("w_in",)), ("swap", ("w_ffn1_out",))),
    "d_ffn1_in_a": (("exchange", ("w_ffn1_out",)),),
    "d_ffn1_in_b": (("swap", ("w_ffn1_in@0",)),),
    "d_ffn1_h": (("exchange", ("w_ffn1_in@0",)), ("swap", ("w_ffn1_in@1",))),
    "bwd_norm1": (("exchange", ("w_ffn1_in@1",)),),
    "small_allreduce": (("join", tuple(n for n in BIG if n != "w_ffn1_in") + ("w_ffn1_in@0", "w_ffn1_in@1")),),
}


def _kname(n):
    base, _, part = n.partition("@")
    return KERNEL_NAME[base] + ("@" + part if part else "")


def _both(comms):
    if len(comms) == 1:
        return comms[0]
    ins, outs, sems, aliases, spans = [], [], [], {}, []
    for c in comms:
        spans.append((len(ins), len(c.ins), len(outs), len(c.out_shapes), len(sems), len(c.sems)))
        aliases.update({len(ins) + i: len(outs) + o for i, o in c.aliases.items()})
        ins, outs, sems = ins + c.ins, outs + c.out_shapes, sems + c.sems

    def each(which):
        def run(i_, o_, s_):
            for c, (ia, ni, oa, no, sa, ns) in zip(comms, spans):
                getattr(c, which)(i_[ia:ia + ni], o_[oa:oa + no], s_[sa:sa + ns])
        return run

    return _Comm(ins, outs, sems, each("start"), each("finish"), aliases)


class _ExchangePlan:
    def __init__(self, shards, cidx):
        self.shards, self.cidx = shards, cidx
        self.W, self.G, self.parts, self.pre, self.reduced, self.joined = {}, {}, {}, {}, {}, {}

    def grad(self, key, g):
        self.G[key] = g

    def rider(self, name):
        comms = []
        for kind, names in RIDES.get(name, ()):
            if kind == "gather":
                comms.append(_gather_comm([self.shards[n] for n in names]))
            elif kind == "swap":
                for n in names:
                    self.parts[n] = self._stacked(n)
                comms.append(_swap_comm([self.parts[n] for n in names]))
            elif kind == "exchange":
                comms.append(_exchange_comm([self.pre[n] for n in names]))
            else:
                comms.append(_join_comm([self.reduced[n] for n in names]))
        return _both(comms) if comms else None

    def landed(self, name, outs):
        at = 0
        for kind, names in RIDES[name]:
            for n, o in zip(names, outs[at:at + len(names)]):
                if kind == "gather":
                    self.W[KERNEL_NAME[n]] = self._to_kernel(n, o)
                elif kind == "swap":
                    self.pre[n] = _add_half(self.parts[n], o, self.cidx, name="rs_add_" + _kname(n))
                elif kind == "exchange":
                    self.reduced[n] = _sum_chips(o, self.cidx, name="rs_sum_" + _kname(n))
                else:
                    self.joined[n] = o
            at += len(names)

    @staticmethod
    def _to_kernel(n, stacked):
        if n in GU:
            return stacked
        if n in TRANSPOSED:
            return _w_in_stacked_to_kernel(stacked, W_IN_SHARD)
        full = _assemble(n, stacked)
        return {"w_q_up": _q_up_to_kernel, "w_kv_up": _kv_up_to_kernel}.get(n, lambda w: w)(full)

    def _stacked(self, n):
        g = self.G[_kname(n)]
        if n.partition("@")[0] in GU:
            return g
        if n in TRANSPOSED:
            return _w_in_kernel_to_stacked(g, W_IN_SHARD)
        full = {"w_q_up": _q_up_from_kernel, "w_kv_up": _kv_up_from_kernel}.get(n, lambda w: w)(g)
        return _split(n, full)


def kernel(x, c, positions, w_ada, b_ada, norm_ffn1, w_ffn1_in, w_ffn1_out, norm_mix, w_in, pool_grp, pool_scale, w_pool_proj, q_a_norm, w_q_up, kv_a_norm, w_kv_up, q_norm_nope, q_norm_rope, k_norm_nope, k_norm_rope, w_mla_proj, w_out, norm_ffn2, w_ffn2_in, w_ffn2_out, loss_target, m_w_ada, m_b_ada, m_norm_ffn1, m_w_ffn1_in, m_w_ffn1_out, m_norm_mix, m_w_in, m_pool_grp, m_pool_scale, m_w_pool_proj, m_q_a_norm, m_w_q_up, m_kv_a_norm, m_w_kv_up, m_q_norm_nope, m_q_norm_rope, m_k_norm_nope, m_k_norm_rope, m_w_mla_proj, m_w_out, m_norm_ffn2, m_w_ffn2_in, m_w_ffn2_out, v_w_ada, v_b_ada, v_norm_ffn1, v_w_ffn1_in, v_w_ffn1_out, v_norm_mix, v_w_in, v_pool_grp, v_pool_scale, v_w_pool_proj, v_q_a_norm, v_w_q_up, v_kv_a_norm, v_w_kv_up, v_q_norm_nope, v_q_norm_rope, v_k_norm_nope, v_k_norm_rope, v_w_mla_proj, v_w_out, v_norm_ffn2, v_w_ffn2_in, v_w_ffn2_out):
    args = dict(locals())
    wts = {n: args[n][0] for n in WEIGHTS}
    mom = {n: args["m_" + n][0] for n in WEIGHTS}
    var = {n: args["v_" + n][0] for n in WEIGHTS}
    nb, seq, dm = x.shape
    tokens = nb * seq
    xi, yi, ci = _coords()
    chip = 2 * xi + yi

    cidx = ci.astype(jnp.int32).reshape(1)
    plan = _ExchangePlan({n: _pad_rows(wts[n].T.astype(MX)) if n in TRANSPOSED else wts[n].astype(MX) for n in BIG}, cidx)
    plan.W["pool_grp"] = wts["pool_grp"].astype(MX)

    ncol = w_ada.shape[2]
    b_cols = lax.dynamic_slice_in_dim(wts["b_ada"].reshape(1, -1), chip * ncol, ncol, axis=1)
    c_slots, mod_slots = _run(plan, _ada_prologue, c, wts["w_ada"], b_cols, name="ada_prologue")
    c_all = c_slots[:, :nb].reshape(N_DEV * nb, dm)
    mod3 = jnp.transpose(mod_slots[:, :nb], (1, 0, 2)).reshape(nb, 9, dm)
    P = {"norm_ffn1": wts["norm_ffn1"].reshape(1, dm), "norm_mix": wts["norm_mix"].reshape(1, dm),
         "norm_ffn2": wts["norm_ffn2"].reshape(1, dm), "pool_scale": wts["pool_scale"].reshape(1, POOL_W),
         "q_a_norm": wts["q_a_norm"].reshape(1, QL), "kv_a_norm": wts["kv_a_norm"].reshape(1, KVL),
         "q_gain": _gain_slab(wts["q_norm_nope"].reshape(1, NOPE), wts["q_norm_rope"].reshape(1, ROPE)),
         "k_gain": _gain_slab(wts["k_norm_nope"].reshape(1, NOPE), wts["k_norm_rope"].reshape(1, ROPE))}
    cos, sin = _rope_tables(positions.reshape(tokens))

    loss8, grad_x, stats, d_pool_grp = _layer_fwd_bwd(x.reshape(tokens, dm), loss_target.reshape(tokens, dm), mod3, cos, sin, plan, P)

    slab = _pack_stats(*stats, loss8, name="pack_stats")
    red, dmod_rows, pool_total = _run(plan, _small_allreduce, slab, d_pool_grp.reshape(POOL_G * LANE, LANE), name="small_allreduce")
    grads_t = {n: plan.joined[n][:W_IN_SHARD] for n in TRANSPOSED}
    grads = {n: grads_t[n].T if n in TRANSPOSED else plan.joined[n] for n in BIG if n != "w_ffn1_in"}
    grads["w_ffn1_in"] = jnp.concatenate([plan.joined["w_ffn1_in@0"], plan.joined["w_ffn1_in@1"]], axis=0)
    dm_rows = dmod_rows.shape[0] // N_DEV
    dmod_all = dmod_rows.reshape(N_DEV, dm_rows, SLAB_W)[:, :9 * nb].reshape(N_DEV * nb, 9 * dm)
    dmod_cols = lax.dynamic_slice_in_dim(dmod_all, chip * ncol, ncol, axis=1)
    grads["w_ada"] = _ada_bwd(c_all, dmod_cols, name="ada_bwd")

    delta, new_m, new_v = {}, {}, {}
    as2d = lambda a: a.reshape(POOL_G * LANE, LANE) if a.ndim == 4 else a.reshape(1, -1)
    upd, loss_row = _small_update(red, dmod_rows, pool_total, nb,
                                  {n: tuple(as2d(args[p + n]) for p in ("", "m_", "v_")) for n in SMALL}, name="small_update")
    loss = loss_row[0, 0]
    for n in SMALL:
        grads[n], delta[n], new_m[n], new_v[n] = upd[n]
    for n in ("w_ada",) + BIG:
        if n in NARROW:
            res = _adamw(wts[n].T, grads_t[n] if n in TRANSPOSED else grads[n].T, mom[n].T, var[n].T, name="adamw_" + n)
            delta[n], new_m[n], new_v[n] = (r.T for r in res)
        else:
            delta[n], new_m[n], new_v[n] = _adamw(wts[n], grads[n], mom[n], var[n], name="adamw_" + n)

    def lead(a, n):
        return a.reshape((1,) + wts[n].shape)

    return (loss, grad_x.reshape(nb, seq, dm), *[lead(grads[n], n) for n in WEIGHTS], *[lead(delta[n], n) for n in WEIGHTS],
            *[lead(new_m[n], n) for n in WEIGHTS], *[lead(new_v[n], n) for n in WEIGHTS])
```

```python
import functools
import math

import jax
import jax.numpy as jnp
from jax import lax
from jax.experimental import pallas as pl
from jax.experimental.pallas import tpu as pltpu

F32 = jnp.float32
MX = jnp.bfloat16

D = 1024
DFF = 2816
NH = 8
POOL_W = 512
POOL_G = 4
QL = 384
KVL = 256
ROPE = 32
NOPE = 64
LANE = 128
SUBLANE = 8
EPS = 1e-6
ATTN_SCALE = 1.0 / math.sqrt(96.0)
NEG = -1e30

Z_UP, Z_QL, Z_KV, Z_KR, Z_GP, Z_GM, Z_W = 0, 512, 896, 1152, 1280, 2304, 3328
KR_LANE = 64
LAT_W = 1280

ADAM_LR, ADAM_B1, ADAM_B2, ADAM_EPS, ADAM_WD, ADAM_STEP = 0.001, 0.9, 0.999, 1e-08, 0.01, 10

VMEM_LIMIT = 48 * 1024 * 1024
MESH = pl.DeviceIdType.MESH
N_DEV = 8
N_CHIP = 4


class _Comm:
    def __init__(self, ins, out_shapes, sems, start, finish, aliases=None):
        self.ins, self.out_shapes, self.sems = list(ins), list(out_shapes), list(sems)
        self.start, self.finish = start, finish
        self.aliases = aliases or {}


def _pcall(body, *, name, out_shape, grid=(), in_specs=None, out_specs=None, scratch=(), grid_spec=None, aliases=None,
           comm=None):
    params = pltpu.CompilerParams(vmem_limit_bytes=VMEM_LIMIT)
    kw = dict(name=name, compiler_params=params)
    if comm is None:
        if aliases:
            kw["input_output_aliases"] = aliases
        if grid_spec is not None:
            return pl.pallas_call(body, grid_spec=grid_spec, out_shape=out_shape, **kw)
        return pl.pallas_call(body, grid=grid, in_specs=in_specs, out_specs=out_specs, scratch_shapes=scratch,
                              out_shape=out_shape, **kw)
    single = not isinstance(out_shape, (list, tuple))
    outs = [out_shape] if single else list(out_shape)
    ospecs = [out_specs] if single else list(out_specs)
    n_in, n_out, n_ci, n_co, n_scr = len(in_specs), len(outs), len(comm.ins), len(comm.out_shapes), len(scratch)
    io = dict(aliases or {})
    io.update({n_in + i: n_out + o for i, o in comm.aliases.items()})

    def riding(*refs):
        ins, cins = refs[:n_in], refs[n_in:n_in + n_ci]
        at = n_in + n_ci
        os_, couts = refs[at:at + n_out], refs[at + n_out:at + n_out + n_co]
        at += n_out + n_co
        scr, csems = refs[at:at + n_scr], refs[at + n_scr:]
        if grid:
            first = functools.reduce(jnp.logical_and, [pl.program_id(d) == 0 for d in range(len(grid))])
            last = functools.reduce(jnp.logical_and, [pl.program_id(d) == grid[d] - 1 for d in range(len(grid))])
            pl.when(first)(lambda: comm.start(cins, couts, csems))
            body(*ins, *os_, *scr)
            pl.when(last)(lambda: comm.finish(cins, couts, csems))
        else:
            comm.start(cins, couts, csems)
            body(*ins, *os_, *scr)
            comm.finish(cins, couts, csems)

    call = pl.pallas_call(riding, grid=grid, in_specs=list(in_specs) + [HBM_SPEC] * n_ci, out_specs=ospecs + [HBM_SPEC] * n_co,
                          out_shape=outs + comm.out_shapes, scratch_shapes=list(scratch) + comm.sems,
                          input_output_aliases=io, **kw)

    def run(*args):
        res = call(*args, *comm.ins)
        main = list(res[:n_out])
        return (main[0] if single else main), list(res[n_out:])

    return run


def _pick(dim, target):
    best = None
    for t in range(LANE, min(dim, target) + 1, LANE):
        if dim % t == 0:
            best = t
    return dim if best is None else best


def _sds(shape, dtype):
    return jax.ShapeDtypeStruct(shape, dtype)


def _dw(a, g, *, name, tm=512, tn=1024, out_t=False, comm=None):
    return _mm(a, g, mode="tn", out_dtype=F32, name=name, tm=tm, tn=tn, tk=a.shape[0], n_outer=True, out_t=out_t, comm=comm)


def _mm(a, b, *, mode, out_dtype, name, tm=1024, tn=1024, tk=4096, n_outer=False, out_t=False, comm=None):
    if mode == "nn":
        (M, K), (K2, N) = a.shape, b.shape
    elif mode == "nt":
        (M, K), (N, K2) = a.shape, b.shape
    else:
        (K, M), (K2, N) = a.shape, b.shape
    assert K == K2, (name, a.shape, b.shape)
    tm, tn, tk = _pick(M, tm), _pick(N, tn), _pick(K, tk)
    nk = K // tk
    if n_outer:
        ij = lambda g0, g1: (g1, g0)
        grid = (N // tn, M // tm, nk)
    else:
        ij = lambda g0, g1: (g0, g1)
        grid = (M // tm, N // tn, nk)
    if mode == "tn":
        a_spec = pl.BlockSpec((tk, tm), lambda g0, g1, k: (k, ij(g0, g1)[0]))
    else:
        a_spec = pl.BlockSpec((tm, tk), lambda g0, g1, k: (ij(g0, g1)[0], k))
    if mode == "nt":
        b_spec = pl.BlockSpec((tn, tk), lambda g0, g1, k: (ij(g0, g1)[1], k))
    else:
        b_spec = pl.BlockSpec((tk, tn), lambda g0, g1, k: (k, ij(g0, g1)[1]))
    if out_t:
        assert nk == 1, name
        o_spec = pl.BlockSpec((tn, tm), lambda g0, g1, k: ij(g0, g1)[::-1])
    else:
        o_spec = pl.BlockSpec((tm, tn), lambda g0, g1, k: ij(g0, g1))
    dn = {"nn": (((1,), (0,)), ((), ())), "nt": (((1,), (1,)), ((), ())), "tn": (((0,), (0,)), ((), ()))}[mode]

    def dot(a_ref, b_ref):
        return lax.dot_general(a_ref[...].astype(MX), b_ref[...].astype(MX), dn, preferred_element_type=F32)

    def body_one(a_ref, b_ref, o_ref):
        prod = dot(a_ref, b_ref)
        o_ref[...] = (prod.T if out_t else prod).astype(o_ref.dtype)

    def body_acc(a_ref, b_ref, o_ref, acc_ref):
        k = pl.program_id(2)
        part = dot(a_ref, b_ref)

        @pl.when(k == 0)
        def _():
            acc_ref[...] = part

        @pl.when(k > 0)
        def _():
            acc_ref[...] += part

        @pl.when(k == nk - 1)
        def _():
            o_ref[...] = acc_ref[...].astype(o_ref.dtype)

    return _pcall(body_one if nk == 1 else body_acc, name=name, out_shape=_sds((N, M) if out_t else (M, N), out_dtype), grid=grid,
                  in_specs=[a_spec, b_spec], out_specs=o_spec, scratch=[] if nk == 1 else [pltpu.VMEM((tm, tn), F32)],
                  comm=comm)(a, b)


def _gu_shard(q):
    return (q % 2) * 2 + q // 2


def _ffn_up(h, w_st, *, name, tm=512, comm=None):
    T, dm = h.shape
    hw = w_st.shape[2]
    tm = _pick(T, tm)

    def body(h_ref, wg_ref, wu_ref, gu_ref, a_ref):
        hv = h_ref[...]
        g = jnp.dot(hv, wg_ref[0], preferred_element_type=F32)
        u = jnp.dot(hv, wu_ref[0], preferred_element_type=F32)
        gu_ref[:, :hw] = g.astype(gu_ref.dtype)
        gu_ref[:, hw:] = u.astype(gu_ref.dtype)
        a_ref[...] = (g * _sigmoid(g) * u).astype(a_ref.dtype)

    return _pcall(body, name=name, grid=(T // tm, 2),
                  in_specs=[pl.BlockSpec((tm, dm), lambda i, j: (i, 0)), pl.BlockSpec((1, dm, hw), lambda i, j: (j, 0, 0)),
                            pl.BlockSpec((1, dm, hw), lambda i, j: (2 + j, 0, 0))],
                  out_specs=[pl.BlockSpec((tm, 2 * hw), lambda i, j: (i, j)), pl.BlockSpec((tm, hw), lambda i, j: (i, j))],
                  out_shape=[_sds((T, 4 * hw), MX), _sds((T, 2 * hw), MX)], comm=comm)(h, w_st, w_st)


def _ffn_up_first(x, mod3, gain, w_st, *, sub, name, tm=512, comm=None):
    T, dm = x.shape
    hw = w_st.shape[2]
    tm = _pick(T, tm)
    tps = (T // mod3.shape[0]) // tm

    def body(x_ref, mod_ref, n_ref, wg_ref, wu_ref, h_ref, gu_ref, a_ref):
        xv = x_ref[...]
        xn = xv * _rsq(xv) * n_ref[...]
        hv = (xn * (1.0 + mod_ref[0, 3 * sub + 1:3 * sub + 2, :]) + mod_ref[0, 3 * sub:3 * sub + 1, :]).astype(h_ref.dtype)
        h_ref[...] = hv
        g = jnp.dot(hv, wg_ref[0], preferred_element_type=F32)
        u = jnp.dot(hv, wu_ref[0], preferred_element_type=F32)
        gu_ref[:, :hw] = g.astype(gu_ref.dtype)
        gu_ref[:, hw:] = u.astype(gu_ref.dtype)
        a_ref[...] = (g * _sigmoid(g) * u).astype(a_ref.dtype)

    return _pcall(body, name=name, grid=(T // tm,),
                  in_specs=[_row_spec(tm, dm), pl.BlockSpec((1, 9, dm), lambda i: (i // tps, 0, 0)),
                            pl.BlockSpec((1, dm), lambda i: (0, 0)), pl.BlockSpec((1, dm, hw), lambda i: (0, 0, 0)),
                            pl.BlockSpec((1, dm, hw), lambda i: (2, 0, 0))],
                  out_specs=[_row_spec(tm, dm), pl.BlockSpec((tm, 2 * hw), lambda i: (i, 0)), pl.BlockSpec((tm, hw), lambda i: (i, 0))],
                  out_shape=[_sds((T, dm), MX), _sds((T, 4 * hw), MX), _sds((T, 2 * hw), MX)],
                  comm=comm)(x, mod3, gain, w_st, w_st)


def _ffn_up_second(h, w_st, gu, a, *, name, tm=512, comm=None):
    T, dm = h.shape
    hw = w_st.shape[2]
    tm = _pick(T, tm)

    def body(h_ref, wg_ref, wu_ref, gu_in, a_in, gu_ref, a_ref):
        hv = h_ref[...]
        g = jnp.dot(hv, wg_ref[0], preferred_element_type=F32)
        u = jnp.dot(hv, wu_ref[0], preferred_element_type=F32)
        gu_ref[:, :hw] = g.astype(gu_ref.dtype)
        gu_ref[:, hw:] = u.astype(gu_ref.dtype)
        a_ref[...] = (g * _sigmoid(g) * u).astype(a_ref.dtype)

    return _pcall(body, name=name, grid=(T // tm,),
                  in_specs=[_row_spec(tm, dm), pl.BlockSpec((1, dm, hw), lambda i: (1, 0, 0)),
                            pl.BlockSpec((1, dm, hw), lambda i: (3, 0, 0)), HBM_SPEC, HBM_SPEC],
                  out_specs=[pl.BlockSpec((tm, 2 * hw), lambda i: (i, 1)), pl.BlockSpec((tm, hw), lambda i: (i, 1))],
                  out_shape=[_sds(gu.shape, gu.dtype), _sds(a.shape, a.dtype)], aliases={3: 0, 4: 1},
                  comm=comm)(h, w_st, w_st, gu, a)


def _ffn_dact(df, gu, w_out, *, name, tm=512, comm=None):
    T, dm = df.shape
    hw = gu.shape[1] // 4
    tm = _pick(T, tm)

    def body(df_ref, gu_ref, wo_ref, dgu_ref):
        da = lax.dot_general(df_ref[...], wo_ref[...], (((1,), (1,)), ((), ())), preferred_element_type=F32)
        g = gu_ref[:, :hw].astype(F32)
        u = gu_ref[:, hw:].astype(F32)
        s = _sigmoid(g)
        dgu_ref[:, :hw] = (da * u * (s * (1.0 + g * (1.0 - s)))).astype(dgu_ref.dtype)
        dgu_ref[:, hw:] = (da * (g * s)).astype(dgu_ref.dtype)

    return _pcall(body, name=name, grid=(T // tm, 2),
                  in_specs=[pl.BlockSpec((tm, dm), lambda i, j: (i, 0)), pl.BlockSpec((tm, 2 * hw), lambda i, j: (i, j)),
                            pl.BlockSpec((hw, dm), lambda i, j: (j, 0))],
                  out_specs=pl.BlockSpec((tm, 2 * hw), lambda i, j: (i, j)), out_shape=_sds(gu.shape, MX),
                  comm=comm)(df, gu, w_out)


def _ffn_dh(dgu, w_st, *, name, tm=1024, comm=None):
    T = dgu.shape[0]
    _, dm, hw = w_st.shape
    tm = _pick(T, tm)

    def body(d_ref, w_ref, o_ref, acc_ref):
        q = pl.program_id(1)
        part = lax.dot_general(d_ref[...], w_ref[0], (((1,), (1,)), ((), ())), preferred_element_type=F32)

        @pl.when(q == 0)
        def _():
            acc_ref[...] = part

        @pl.when(jnp.logical_and(q > 0, q < 3))
        def _():
            acc_ref[...] += part

        @pl.when(q == 3)
        def _():
            o_ref[...] = acc_ref[...] + part

    return _pcall(body, name=name, grid=(T // tm, 4),
                  in_specs=[pl.BlockSpec((tm, hw), lambda i, q: (i, q)), pl.BlockSpec((1, dm, hw), lambda i, q: (_gu_shard(q), 0, 0))],
                  out_specs=pl.BlockSpec((tm, dm), lambda i, q: (i, 0)), out_shape=_sds((T, dm), F32),
                  scratch=[pltpu.VMEM((tm, dm), F32)], comm=comm)(dgu, w_st)


def _ffn_dw_in(h, dgu, *, name, rows=None, tm=512, comm=None):
    T, dm = h.shape
    hw = dgu.shape[1] // 4
    first, count = rows if rows is not None else (0, dm)
    tm = _pick(count, tm)
    skip = first // tm

    def body(h_ref, d_ref, o_ref):
        o_ref[0] = lax.dot_general(h_ref[...], d_ref[...], (((0,), (0,)), ((), ())), preferred_element_type=F32)

    return _pcall(body, name=name, grid=(4, count // tm),
                  in_specs=[pl.BlockSpec((T, tm), lambda q, i: (0, skip + i)), pl.BlockSpec((T, hw), lambda q, i: (0, q))],
                  out_specs=pl.BlockSpec((1, tm, hw), lambda q, i: (_gu_shard(q), i, 0)),
                  out_shape=_sds((4, count, hw), F32), comm=comm)(h, dgu)


def _rsq(x):
    return lax.rsqrt(jnp.mean(x * x, axis=-1, keepdims=True) + EPS)


def _sigmoid(x):
    return 1.0 / (1.0 + jnp.exp(-x))


def _row_spec(tm, w):
    return pl.BlockSpec((tm, w), lambda i: (i, 0))


def _mm_resid_norm(a, w, x_prev, mod3, gain, *, sub, coef, name, tm=512, comm=None):
    T, k = a.shape
    dm = w.shape[1]
    tm = _pick(T, tm)
    tps = (T // mod3.shape[0]) // tm

    def body(a_ref, w_ref, x_ref, mod_ref, n_ref, f_ref, xo_ref, h_ref):
        f = jnp.dot(a_ref[...], w_ref[...], preferred_element_type=F32)
        f_ref[...] = f
        x = x_ref[...] + coef * mod_ref[0, 3 * sub - 1:3 * sub, :] * f
        xo_ref[...] = x
        xn = x * _rsq(x) * n_ref[...]
        h_ref[...] = (xn * (1.0 + mod_ref[0, 3 * sub + 1:3 * sub + 2, :]) + mod_ref[0, 3 * sub:3 * sub + 1, :]).astype(h_ref.dtype)

    row = _row_spec(tm, dm)
    return _pcall(body, name=name, grid=(T // tm,),
                  in_specs=[_row_spec(tm, k), pl.BlockSpec((k, dm), lambda i: (0, 0)), row,
                            pl.BlockSpec((1, 9, dm), lambda i: (i // tps, 0, 0)), pl.BlockSpec((1, dm), lambda i: (0, 0))],
                  out_specs=[row, row, row], out_shape=[_sds((T, dm), F32), _sds((T, dm), F32), _sds((T, dm), MX)],
                  comm=comm)(a, w, x_prev, mod3, gain)


def _mm_loss(a, w, x2, tgt, mod3, *, name, tm=512):
    T, k = a.shape
    dm = w.shape[1]
    tm = _pick(T, tm)
    tps = (T // mod3.shape[0]) // tm
    mod_spec = pl.BlockSpec((1, 9, dm), lambda i: (i // tps, 0, 0))
    row = _row_spec(tm, dm)
    stat_spec = pl.BlockSpec((1, SUBLANE, dm), lambda i: (i // tps, 0, 0))
    loss_spec = pl.BlockSpec((SUBLANE, LANE), lambda i: (0, 0))

    def body(a_ref, w_ref, x_ref, t_ref, mod_ref, dy_ref, df_ref, st_ref, loss_ref):
        i = pl.program_id(0)
        g = mod_ref[0, 8:9, :]
        f = jnp.dot(a_ref[...], w_ref[...], preferred_element_type=F32)
        err = x_ref[...] + 0.5 * g * f - t_ref[...]
        dy = err * (1.0 / dm)
        dy_ref[...] = dy
        df_ref[...] = (0.5 * g * dy).astype(df_ref.dtype)
        dgate = jnp.sum(0.5 * dy * f, axis=0, keepdims=True)
        part = 0.5 * jnp.sum(jnp.sum(err * err, axis=0, keepdims=True), axis=1, keepdims=True) * (1.0 / dm)

        @pl.when(i % tps == 0)
        def _():
            st_ref[...] = jnp.zeros_like(st_ref)

        @pl.when(i == 0)
        def _():
            loss_ref[...] = jnp.zeros_like(loss_ref)

        st_ref[0, 0:1, :] += dgate
        loss_ref[...] += jnp.broadcast_to(part, loss_ref.shape)

    return _pcall(body, name=name, grid=(T // tm,),
                  in_specs=[_row_spec(tm, k), pl.BlockSpec((k, dm), lambda i: (0, 0)), row, row, mod_spec],
                  out_specs=[row, row, stat_spec, loss_spec],
                  out_shape=[_sds((T, dm), F32), _sds((T, dm), MX), _sds((mod3.shape[0], SUBLANE, dm), F32),
                             _sds((SUBLANE, LANE), F32)])(a, w, x2, tgt, mod3)


def _norm_bwd_tail(dhv, x_ref, dxi_ref, f_ref, mod_ref, n_ref, dx_ref, df_ref, st_ref, *, first, sub, coef):
    x = x_ref[...]
    r = _rsq(x)
    xhat = x * r
    n = n_ref[...]
    d_shift = jnp.sum(dhv, axis=0, keepdims=True)
    d_scale = jnp.sum(dhv * (xhat * n), axis=0, keepdims=True)
    dxn = dhv * (1.0 + mod_ref[0, 3 * sub + 1:3 * sub + 2, :])
    d_gain = jnp.sum(dxn * xhat, axis=0, keepdims=True)
    dxhat = dxn * n
    dx = dxi_ref[...] + r * (dxhat - xhat * jnp.mean(dxhat * xhat, axis=-1, keepdims=True))
    dx_ref[...] = dx

    @pl.when(first)
    def _():
        st_ref[...] = jnp.zeros_like(st_ref)

    st_ref[0, 0:1, :] += d_shift
    st_ref[0, 1:2, :] += d_scale
    st_ref[0, 2:3, :] += d_gain
    if f_ref is not None:
        st_ref[0, 3:4, :] += jnp.sum(coef * dx * f_ref[...], axis=0, keepdims=True)
        df_ref[...] = (coef * mod_ref[0, 3 * sub - 1:3 * sub, :] * dx).astype(df_ref.dtype)


def _ffn_dh_norm(dgu, w_st, x_cur, dx_in, f_prev, mod3, gain, *, sub, coef, name, tm=512, comm=None):
    T = dgu.shape[0]
    _, dm, hw = w_st.shape
    tm = _pick(T, tm)
    nb = mod3.shape[0]
    tps = (T // nb) // tm

    def body(d_ref, w_ref, x_ref, dxi_ref, f_ref, mod_ref, n_ref, dx_ref, df_ref, st_ref, acc_ref):
        i = pl.program_id(0)
        q = pl.program_id(1)
        part = lax.dot_general(d_ref[...], w_ref[0], (((1,), (1,)), ((), ())), preferred_element_type=F32)

        @pl.when(q == 0)
        def _():
            acc_ref[...] = part

        @pl.when(q > 0)
        def _():
            acc_ref[...] += part

        @pl.when(q == 3)
        def _():
            _norm_bwd_tail(acc_ref[...], x_ref, dxi_ref, f_ref, mod_ref, n_ref, dx_ref, df_ref, st_ref,
                           first=i % tps == 0, sub=sub, coef=coef)

    row = pl.BlockSpec((tm, dm), lambda i, q: (i, 0))
    return _pcall(body, name=name, grid=(T // tm, 4),
                  in_specs=[pl.BlockSpec((tm, hw), lambda i, q: (i, q)), pl.BlockSpec((1, dm, hw), lambda i, q: (_gu_shard(q), 0, 0)),
                            row, row, row, pl.BlockSpec((1, 9, dm), lambda i, q: (i // tps, 0, 0)),
                            pl.BlockSpec((1, dm), lambda i, q: (0, 0))],
                  out_specs=[row, row, pl.BlockSpec((1, SUBLANE, dm), lambda i, q: (i // tps, 0, 0))],
                  out_shape=[_sds((T, dm), F32), _sds((T, dm), MX), _sds((nb, SUBLANE, dm), F32)],
                  scratch=[pltpu.VMEM((tm, dm), F32)], comm=comm)(dgu, w_st, x_cur, dx_in, f_prev, mod3, gain)


def _mm_norm_bwd(a, b, x_cur, dx_in, f_prev, mod3, gain, *, sub, coef, name, tm=512, comm=None):
    T, k = a.shape
    dm = b.shape[1]
    tm = _pick(T, tm)
    nb = mod3.shape[0]
    tps = (T // nb) // tm

    def body(a_ref, b_ref, x_ref, dxi_ref, f_ref, mod_ref, n_ref, dx_ref, df_ref, st_ref):
        dh = jnp.dot(a_ref[...], b_ref[...], preferred_element_type=F32)
        _norm_bwd_tail(dh, x_ref, dxi_ref, f_ref, mod_ref, n_ref, dx_ref, df_ref, st_ref,
                       first=pl.program_id(0) % tps == 0, sub=sub, coef=coef)

    row = _row_spec(tm, dm)
    return _pcall(body, name=name, grid=(T // tm,),
                  in_specs=[_row_spec(tm, k), pl.BlockSpec((k, dm), lambda i: (0, 0)), row, row, row,
                            pl.BlockSpec((1, 9, dm), lambda i: (i // tps, 0, 0)), pl.BlockSpec((1, dm), lambda i: (0, 0))],
                  out_specs=[row, row, pl.BlockSpec((1, SUBLANE, dm), lambda i: (i // tps, 0, 0))],
                  out_shape=[_sds((T, dm), F32), _sds((T, dm), MX), _sds((nb, SUBLANE, dm), F32)],
                  comm=comm)(a, b, x_cur, dx_in, f_prev, mod3, gain)


def _bwd_block(x_cur, dh, dx_in, f_prev, mod3, gain, *, sub, coef, name, tm=256, comm=None):
    T, dm = x_cur.shape
    nb = mod3.shape[0]
    tps = (T // nb) // tm
    has_f = f_prev is not None
    mod_spec = pl.BlockSpec((1, 9, dm), lambda i: (i // tps, 0, 0))
    vec_spec = pl.BlockSpec((1, dm), lambda i: (0, 0))
    stat_spec = pl.BlockSpec((1, SUBLANE, dm), lambda i: (i // tps, 0, 0))
    row = _row_spec(tm, dm)

    def body(*refs):
        if has_f:
            x_ref, dh_ref, dxi_ref, f_ref, mod_ref, n_ref, dx_ref, df_ref, st_ref = refs
        else:
            x_ref, dh_ref, dxi_ref, mod_ref, n_ref, dx_ref, st_ref = refs
            f_ref = df_ref = None
        _norm_bwd_tail(dh_ref[...], x_ref, dxi_ref, f_ref, mod_ref, n_ref, dx_ref, df_ref, st_ref,
                       first=pl.program_id(0) % tps == 0, sub=sub, coef=coef)

    st_shape = _sds((nb, SUBLANE, dm), F32)
    if has_f:
        return _pcall(body, name=name, grid=(T // tm,), in_specs=[row, row, row, row, mod_spec, vec_spec],
                      out_specs=[row, row, stat_spec],
                      out_shape=[_sds((T, dm), F32), _sds((T, dm), MX), st_shape], comm=comm)(x_cur, dh, dx_in, f_prev, mod3, gain)
    return _pcall(body, name=name, grid=(T // tm,), in_specs=[row, row, row, mod_spec, vec_spec],
                  out_specs=[row, stat_spec], out_shape=[_sds((T, dm), F32), st_shape], comm=comm)(x_cur, dh, dx_in, mod3, gain)


def _mix_out_fwd(z, br_pool, attn, w_mla, w_out, x_prev, mod3, gain, *, sub, name, tm=512):
    T = z.shape[0]
    dm = w_out.shape[1]
    tm = _pick(T, tm)
    tps = (T // mod3.shape[0]) // tm
    whole = lambda a: pl.BlockSpec(a.shape, lambda i: (0, 0))

    def body(z_ref, bp_ref, at_ref, wm_ref, wo_ref, x_ref, mod_ref, n_ref, bm_ref, mg_ref, mo_ref, xo_ref, h_ref):
        bm = jnp.dot(at_ref[...], wm_ref[...], preferred_element_type=F32).astype(MX)
        bm_ref[...] = bm
        gp = z_ref[:, Z_GP:Z_GP + D].astype(F32)
        gm = z_ref[:, Z_GM:Z_GM + D].astype(F32)
        merged = (_sigmoid(gp) * bp_ref[...] + _sigmoid(gm) * bm).astype(MX)
        mg_ref[...] = merged
        mo = jnp.dot(merged, wo_ref[...], preferred_element_type=F32)
        mo_ref[...] = mo
        x = x_ref[...] + mod_ref[0, 3 * sub - 1:3 * sub, :] * mo
        xo_ref[...] = x
        xn = x * _rsq(x) * n_ref[...]
        h_ref[...] = (xn * (1.0 + mod_ref[0, 3 * sub + 1:3 * sub + 2, :]) + mod_ref[0, 3 * sub:3 * sub + 1, :]).astype(h_ref.dtype)

    row = _row_spec(tm, dm)
    return _pcall(body, name=name, grid=(T // tm,),
                  in_specs=[_row_spec(tm, Z_W), row, _row_spec(tm, attn.shape[1]), whole(w_mla), whole(w_out), row,
                            pl.BlockSpec((1, 9, dm), lambda i: (i // tps, 0, 0)), pl.BlockSpec((1, dm), lambda i: (0, 0))],
                  out_specs=[row, row, row, row, row],
                  out_shape=[_sds((T, dm), MX), _sds((T, dm), MX), _sds((T, dm), F32), _sds((T, dm), F32), _sds((T, dm), MX)],
                  )(z, br_pool, attn, w_mla, w_out, x_prev, mod3, gain)


def _mix_out_bwd(dmo, w_out, z, br_pool, br_mla, w_mla, *, name, tm=512):
    T = z.shape[0]
    tm = _pick(T, tm)
    whole = lambda a: pl.BlockSpec(a.shape, lambda i: (0, 0))
    nt = (((1,), (1,)), ((), ()))

    def body(dmo_ref, wo_ref, z_ref, bp_ref, bm_ref, wm_ref, dbp_ref, dbm_ref, dg_ref, dat_ref):
        dm = lax.dot_general(dmo_ref[...], wo_ref[...], nt, preferred_element_type=F32)
        sp = _sigmoid(z_ref[:, Z_GP:Z_GP + D].astype(F32))
        sm = _sigmoid(z_ref[:, Z_GM:Z_GM + D].astype(F32))
        dbp_ref[...] = (dm * sp).astype(dbp_ref.dtype)
        dbm = (dm * sm).astype(MX)
        dbm_ref[...] = dbm
        dg_ref[:, :D] = (dm * bp_ref[...].astype(F32) * sp * (1.0 - sp)).astype(dg_ref.dtype)
        dg_ref[:, D:] = (dm * bm_ref[...].astype(F32) * sm * (1.0 - sm)).astype(dg_ref.dtype)
        dat_ref[...] = lax.dot_general(dbm, wm_ref[...], nt, preferred_element_type=F32)

    row = _row_spec(tm, D)
    return _pcall(body, name=name, grid=(T // tm,),
                  in_specs=[row, whole(w_out), _row_spec(tm, Z_W), row, row, whole(w_mla)],
                  out_specs=[row, row, _row_spec(tm, 2 * D), _row_spec(tm, w_mla.shape[0])],
                  out_shape=[_sds((T, D), MX), _sds((T, D), MX), _sds((T, 2 * D), MX), _sds((T, w_mla.shape[0]), F32)],
                  )(dmo, w_out, z, br_pool, br_mla, w_mla)


def _shift_down(x, k, row):
    return jnp.where(row >= k, pltpu.roll(x, k, 0), 0.0)


def _shift_up(x, k, row, n):
    return jnp.where(row < n - k, pltpu.roll(x, n - k, 0), 0.0)


def _pool_fwd(z, pool_grp, pool_scale, *, nb, name):
    T = z.shape[0]
    S = T // nb
    blk = pl.BlockSpec((S, LANE), lambda b, g: (b, g))

    def body(u_ref, w_ref, s_ref, pooled_ref, mixed_ref, scaled_ref):
        g = pl.program_id(1)
        u = u_ref[...].astype(F32)
        row = lax.broadcasted_iota(jnp.int32, u.shape, 0)
        s2 = u + _shift_down(u, 1, row)
        s4 = s2 + _shift_down(s2, 2, row)
        s8 = s4 + _shift_down(s4, 4, row)
        s16 = s8 + _shift_down(s8, 8, row)
        win = jnp.where(g == 0, s2, jnp.where(g == 1, s4, jnp.where(g == 2, s8, s16)))
        width = lax.shift_left(jnp.int32(2), g)
        cnt = jnp.minimum(row + 1, width).astype(F32)
        pooled = (win / cnt - u).astype(MX)
        pooled_ref[...] = pooled
        mixed = jnp.dot(pooled, w_ref[0], preferred_element_type=F32)
        mixed_ref[...] = mixed
        scaled_ref[...] = (mixed * s_ref[...]).astype(scaled_ref.dtype)

    return _pcall(body, name=name, grid=(nb, POOL_G),
                  in_specs=[blk, pl.BlockSpec((1, LANE, LANE), lambda b, g: (g, 0, 0)),
                            pl.BlockSpec((1, LANE), lambda b, g: (0, g))],
                  out_specs=[blk, blk, blk],
                  out_shape=[_sds((T, POOL_W), MX), _sds((T, POOL_W), F32), _sds((T, POOL_W), MX)])(z, pool_grp, pool_scale)


def _pool_bwd(dscaled, mixed, pooled, pool_grp, pool_scale, *, nb, name):
    T = dscaled.shape[0]
    S = T // nb
    blk = pl.BlockSpec((S, LANE), lambda g, b: (b, g))

    def body(ds_ref, mixed_ref, pooled_ref, w_ref, s_ref, du_ref, dw_ref, dsc_ref):
        g = pl.program_id(0)
        b = pl.program_id(1)
        ds = ds_ref[...]
        dsc_ref[0] = jnp.sum(ds * mixed_ref[...], axis=0, keepdims=True)
        dmixed = (ds * s_ref[...]).astype(MX)
        dw = lax.dot_general(pooled_ref[...], dmixed, (((0,), (0,)), ((), ())), preferred_element_type=F32)

        @pl.when(b == 0)
        def _():
            dw_ref[0] = dw

        @pl.when(b > 0)
        def _():
            dw_ref[0] += dw
        dpooled = lax.dot_general(dmixed, w_ref[0], (((1,), (1,)), ((), ())), preferred_element_type=F32)
        row = lax.broadcasted_iota(jnp.int32, dpooled.shape, 0)
        width = lax.shift_left(jnp.int32(2), g)
        q = dpooled / jnp.minimum(row + 1, width).astype(F32)
        r2 = q + _shift_up(q, 1, row, S)
        r4 = r2 + _shift_up(r2, 2, row, S)
        r8 = r4 + _shift_up(r4, 4, row, S)
        r16 = r8 + _shift_up(r8, 8, row, S)
        win = jnp.where(g == 0, r2, jnp.where(g == 1, r4, jnp.where(g == 2, r8, r16)))
        du_ref[...] = (win - dpooled).astype(du_ref.dtype)

    return _pcall(body, name=name, grid=(POOL_G, nb),
                  in_specs=[blk, blk, blk, pl.BlockSpec((1, LANE, LANE), lambda g, b: (g, 0, 0)),
                            pl.BlockSpec((1, LANE), lambda g, b: (0, g))],
                  out_specs=[blk, pl.BlockSpec((1, LANE, LANE), lambda g, b: (g, 0, 0)),
                             pl.BlockSpec((1, 1, LANE), lambda g, b: (b, 0, g))],
                  out_shape=[_sds((T, POOL_W), MX), _sds((POOL_G, LANE, LANE), F32), _sds((nb, 1, POOL_W), F32)],
                  )(dscaled, mixed, pooled, pool_grp, pool_scale)


def _lane_masks(shape):
    lane = lax.broadcasted_iota(jnp.int32, shape, len(shape) - 1)
    m_n = lane < NOPE
    m_r = jnp.logical_and(lane >= KR_LANE, lane < KR_LANE + ROPE)
    first_half = lane < KR_LANE + ROPE // 2
    return m_n, m_r, first_half


def _rot(y, first_half):
    return jnp.where(first_half, -pltpu.roll(y, LANE - ROPE // 2, 1), pltpu.roll(y, ROPE // 2, 1))


def _rot_t(v, first_half, m_r):
    return jnp.where(m_r, jnp.where(first_half, pltpu.roll(v, LANE - ROPE // 2, 1), -pltpu.roll(v, ROPE // 2, 1)), 0.0)


def _mla_in_fwd(z, w_q, w_kv, qa_gain, kva_gain, cos, sin, q_gain, k_gain, *, name, tm=256):
    T = z.shape[0]
    slab = pl.BlockSpec((tm, LANE), lambda i: (i, 0))
    vec = pl.BlockSpec((1, LANE), lambda i: (0, 0))
    whole = lambda a: pl.BlockSpec(a.shape, lambda i: (0, 0))

    def body(z_ref, wq_ref, wkv_ref, qa_ref, kva_ref, cos_ref, sin_ref, qg_ref, kg_ref,
             qn_ref, kvn_ref, qp_ref, kvp_ref, q_ref, k_ref, v_ref):
        ql = z_ref[:, Z_QL:Z_QL + QL].astype(F32)
        kvl = z_ref[:, Z_KV:Z_KV + KVL].astype(F32)
        qn = (ql * _rsq(ql) * qa_ref[...]).astype(MX)
        kvn = (kvl * _rsq(kvl) * kva_ref[...]).astype(MX)
        qn_ref[...] = qn
        kvn_ref[...] = kvn
        qp = jnp.dot(qn, wq_ref[...], preferred_element_type=F32)
        kvp = jnp.dot(kvn, wkv_ref[...], preferred_element_type=F32)
        qp_ref[...] = qp
        kvp_ref[...] = kvp
        m_n, m_r, first_half = _lane_masks((tm, LANE))
        c = cos_ref[...]
        s = sin_ref[...]
        qg = qg_ref[...]
        kg = kg_ref[...]
        xr = z_ref[:, Z_KR:Z_KR + LANE].astype(F32)
        rr = lax.rsqrt(jnp.sum(xr * xr, axis=-1, keepdims=True) * (1.0 / ROPE) + EPS)
        yr = xr * rr * kg
        kr = jnp.where(m_r, yr * c + _rot(yr, first_half) * s, 0.0)
        for h in range(NH):
            x = qp[:, h * LANE:(h + 1) * LANE]
            x2 = x * x
            rn = lax.rsqrt(jnp.sum(jnp.where(m_n, x2, 0.0), axis=-1, keepdims=True) * (1.0 / NOPE) + EPS)
            rq = lax.rsqrt(jnp.sum(jnp.where(m_r, x2, 0.0), axis=-1, keepdims=True) * (1.0 / ROPE) + EPS)
            y = x * jnp.where(m_n, rn, jnp.where(m_r, rq, 0.0)) * qg
            q_ref[:, h * LANE:(h + 1) * LANE] = ((y * c + _rot(y, first_half) * s) * ATTN_SCALE).astype(q_ref.dtype)
            xk = kvp[:, h * LANE:(h + 1) * LANE]
            rk = lax.rsqrt(jnp.sum(jnp.where(m_n, xk * xk, 0.0), axis=-1, keepdims=True) * (1.0 / NOPE) + EPS)
            k_ref[:, h * LANE:(h + 1) * LANE] = (jnp.where(m_n, xk * rk * kg, 0.0) + kr).astype(k_ref.dtype)
        v_ref[...] = kvp[:, NH * LANE:].astype(v_ref.dtype)

    return _pcall(body, name=name, grid=(T // tm,),
                  in_specs=[_row_spec(tm, LAT_W), whole(w_q), whole(w_kv), whole(qa_gain), whole(kva_gain), slab, slab, vec, vec],
                  out_specs=[_row_spec(tm, QL), _row_spec(tm, KVL), _row_spec(tm, NH * LANE), _row_spec(tm, NH * LANE + NH * NOPE),
                             _row_spec(tm, NH * LANE), _row_spec(tm, NH * LANE), _row_spec(tm, NH * NOPE)],
                  out_shape=[_sds((T, QL), MX), _sds((T, KVL), MX), _sds((T, NH * LANE), F32), _sds((T, NH * LANE + NH * NOPE), F32),
                             _sds((T, NH * LANE), MX), _sds((T, NH * LANE), MX), _sds((T, NH * NOPE), MX)],
                  )(z, w_q, w_kv, qa_gain, kva_gain, cos, sin, q_gain, k_gain)


def _mla_in_bwd(dq, dk, dv, qp, kvp, z, w_q, w_kv, qa_gain, kva_gain, cos, sin, q_gain, k_gain, *, nb, name, tm=256):
    T = dq.shape[0]
    tps = (T // nb) // tm
    slab = pl.BlockSpec((tm, LANE), lambda i: (i, 0))
    vec = pl.BlockSpec((1, LANE), lambda i: (0, 0))
    whole = lambda a: pl.BlockSpec(a.shape, lambda i: (0, 0))

    def latent_bwd(x, dy, gain):
        r = _rsq(x)
        xhat = x * r
        dxhat = dy * gain
        return r * (dxhat - xhat * jnp.mean(dxhat * xhat, axis=-1, keepdims=True)), jnp.sum(dy * xhat, axis=0, keepdims=True)

    def body(dq_ref, dk_ref, dv_ref, qp_ref, kvp_ref, z_ref, wq_ref, wkv_ref, qa_ref, kva_ref, cos_ref, sin_ref, qg_ref, kg_ref,
             dqp_ref, dkvp_ref, dkr_ref, dql_ref, dkvl_ref, st_ref, sq_ref, sk_ref):
        i = pl.program_id(0)
        qp, kvp = qp_ref, kvp_ref
        m_n, m_r, first_half = _lane_masks((tm, LANE))
        c = cos_ref[...]
        s = sin_ref[...]
        qg = qg_ref[...]
        kg = kg_ref[...]
        dqg = jnp.zeros((1, LANE), F32)
        dkg = jnp.zeros((1, LANE), F32)
        dkr_sum = jnp.zeros((tm, LANE), F32)
        for h in range(NH):
            x = qp[:, h * LANE:(h + 1) * LANE]
            x2 = x * x
            rn = lax.rsqrt(jnp.sum(jnp.where(m_n, x2, 0.0), axis=-1, keepdims=True) * (1.0 / NOPE) + EPS)
            rq = lax.rsqrt(jnp.sum(jnp.where(m_r, x2, 0.0), axis=-1, keepdims=True) * (1.0 / ROPE) + EPS)
            rfac = jnp.where(m_n, rn, jnp.where(m_r, rq, 0.0))
            xhat = x * rfac
            do = dq_ref[:, h * LANE:(h + 1) * LANE] * ATTN_SCALE
            dy = do * c + _rot_t(do * s, first_half, m_r)
            dqg = dqg + jnp.sum(dy * xhat, axis=0, keepdims=True)
            dxhat = dy * qg
            t = dxhat * xhat
            mean_n = jnp.sum(jnp.where(m_n, t, 0.0), axis=-1, keepdims=True) * (1.0 / NOPE)
            mean_r = jnp.sum(jnp.where(m_r, t, 0.0), axis=-1, keepdims=True) * (1.0 / ROPE)
            dqp_ref[:, h * LANE:(h + 1) * LANE] = (
                rfac * (dxhat - xhat * jnp.where(m_n, mean_n, jnp.where(m_r, mean_r, 0.0)))).astype(dqp_ref.dtype)

            xk = kvp[:, h * LANE:(h + 1) * LANE]
            rk = lax.rsqrt(jnp.sum(jnp.where(m_n, xk * xk, 0.0), axis=-1, keepdims=True) * (1.0 / NOPE) + EPS)
            khat = jnp.where(m_n, xk * rk, 0.0)
            dko = dk_ref[:, h * LANE:(h + 1) * LANE]
            dkn = jnp.where(m_n, dko, 0.0)
            dkg = dkg + jnp.sum(dkn * khat, axis=0, keepdims=True)
            dkhat = dkn * kg
            mean_k = jnp.sum(dkhat * khat, axis=-1, keepdims=True) * (1.0 / NOPE)
            dkvp_ref[:, h * LANE:(h + 1) * LANE] = jnp.where(m_n, rk * (dkhat - khat * mean_k), 0.0).astype(dkvp_ref.dtype)
            dkr_sum = dkr_sum + jnp.where(m_r, dko, 0.0)
        dkvp_ref[:, NH * LANE:] = dv_ref[...].astype(dkvp_ref.dtype)

        xr = z_ref[:, Z_KR:Z_KR + LANE].astype(F32)
        rr = lax.rsqrt(jnp.sum(xr * xr, axis=-1, keepdims=True) * (1.0 / ROPE) + EPS)
        rhat = xr * rr
        dyr = dkr_sum * c + _rot_t(dkr_sum * s, first_half, m_r)
        dkg = dkg + jnp.sum(dyr * rhat, axis=0, keepdims=True)
        drhat = dyr * kg
        mean_kr = jnp.sum(drhat * rhat, axis=-1, keepdims=True) * (1.0 / ROPE)
        dkr_ref[...] = jnp.where(m_r, rr * (drhat - rhat * mean_kr), 0.0).astype(dkr_ref.dtype)

        nt = (((1,), (1,)), ((), ()))
        dqn = lax.dot_general(dqp_ref[...], wq_ref[...], nt, preferred_element_type=F32)
        dkvn = lax.dot_general(dkvp_ref[...], wkv_ref[...], nt, preferred_element_type=F32)
        dql, dqa = latent_bwd(z_ref[:, Z_QL:Z_QL + QL].astype(F32), dqn, qa_ref[...])
        dkvl, dkva = latent_bwd(z_ref[:, Z_KV:Z_KV + KVL].astype(F32), dkvn, kva_ref[...])
        dql_ref[...] = dql.astype(dql_ref.dtype)
        dkvl_ref[...] = dkvl.astype(dkvl_ref.dtype)

        @pl.when(i % tps == 0)
        def _():
            st_ref[...] = jnp.zeros_like(st_ref)
            sq_ref[...] = jnp.zeros_like(sq_ref)
            sk_ref[...] = jnp.zeros_like(sk_ref)

        st_ref[0, 0:1, :] += dqg
        st_ref[0, 1:2, :] += dkg
        sq_ref[0, 0:1, :] += dqa
        sk_ref[0, 0:1, :] += dkva

    stat = lambda w: pl.BlockSpec((1, SUBLANE, w), lambda i: (i // tps, 0, 0))
    return _pcall(body, name=name, grid=(T // tm,),
                  in_specs=[_row_spec(tm, NH * LANE), _row_spec(tm, NH * LANE), _row_spec(tm, NH * NOPE),
                            _row_spec(tm, NH * LANE), _row_spec(tm, NH * LANE + NH * NOPE), _row_spec(tm, LAT_W),
                            whole(w_q), whole(w_kv), whole(qa_gain), whole(kva_gain), slab, slab, vec, vec],
                  out_specs=[_row_spec(tm, NH * LANE), _row_spec(tm, NH * LANE + NH * NOPE), slab, _row_spec(tm, QL),
                             _row_spec(tm, KVL), stat(LANE), stat(QL), stat(KVL)],
                  out_shape=[_sds((T, NH * LANE), MX), _sds((T, NH * LANE + NH * NOPE), MX), _sds((T, LANE), MX),
                             _sds((T, QL), MX), _sds((T, KVL), MX), _sds((nb, SUBLANE, LANE), F32),
                             _sds((nb, SUBLANE, QL), F32), _sds((nb, SUBLANE, KVL), F32)],
                  )(dq, dk, dv, qp, kvp, z, w_q, w_kv, qa_gain, kva_gain, cos, sin, q_gain, k_gain)


def _lower_triangle(t, keys_first=False):
    key_axis = 0 if keys_first else 1
    return lax.broadcasted_iota(jnp.int32, (t, t), key_axis) <= lax.broadcasted_iota(jnp.int32, (t, t), 1 - key_axis)


def _attn_fwd(q, k, v, *, nb, name, tq=512, comm=None):
    T = q.shape[0]
    S = T // nb
    tq = _pick(S, tq)
    nq = S // tq
    tk = tq
    npair = NH // 2

    def body(q_ref, k_ref, v_ref, o_ref, lse_ref):
        qi = pl.program_id(2)
        lane = lax.broadcasted_iota(jnp.int32, (tq, LANE), 1)
        qs = [q_ref[:, hh * LANE:(hh + 1) * LANE] for hh in range(2)]

        def block(j, carry, diagonal):
            k0 = pl.multiple_of(j * tk, tk)
            vb = v_ref[pl.ds(k0, tk), :]
            vbs = [jnp.where(lane < NOPE, vb, jnp.ones_like(vb)), jnp.where(lane < NOPE, jnp.ones_like(vb), vb)]
            new = []
            for hh in range(2):
                m, acc = carry[hh]
                kb = k_ref[pl.ds(k0, tk), hh * LANE:(hh + 1) * LANE]
                s = lax.dot_general(qs[hh], kb, (((1,), (1,)), ((), ())), preferred_element_type=F32)
                if diagonal:
                    s = jnp.where(_lower_triangle(tq), s, NEG)
                m_new = jnp.maximum(m, jnp.max(s, axis=-1, keepdims=True))
                p = jnp.exp((s - m_new).astype(MX))
                acc = jnp.exp(m - m_new) * acc + jnp.dot(p, vbs[hh], preferred_element_type=F32)
                new.append((m_new, acc))
            return tuple(new)

        init = tuple((jnp.full((tq, 1), NEG, F32), jnp.zeros((tq, LANE), F32)) for _ in range(2))
        carry = lax.fori_loop(0, qi, lambda j, c: block(j, c, False), init)
        (m0, acc0), (m1, acc1) = block(qi, carry, True)
        l0, l1 = acc0[:, NOPE:NOPE + 1], acc1[:, 0:1]
        o_ref[...] = jnp.where(lane < NOPE, acc0 / l0, acc1 / l1).astype(o_ref.dtype)
        lse_ref[...] = jnp.where(lane < NOPE, m0 + jnp.log(l0), m1 + jnp.log(l1))

    return _pcall(body, name=name, grid=(nb, npair, nq),
                  in_specs=[pl.BlockSpec((tq, 2 * LANE), lambda b, p, i: (b * nq + i, p)),
                            pl.BlockSpec((S, 2 * LANE), lambda b, p, i: (b, p)),
                            pl.BlockSpec((S, LANE), lambda b, p, i: (b, p))],
                  out_specs=[pl.BlockSpec((tq, LANE), lambda b, p, i: (b * nq + i, p)),
                             pl.BlockSpec((tq, LANE), lambda b, p, i: (b * nq + i, p))],
                  out_shape=[_sds((T, NH * NOPE), MX), _sds((T, NH * NOPE), F32)], comm=comm)(q, k, v)


def _attn_bwd(q, k, v, o, lse, do, *, nb, name, tq=512, comm=None):
    T = q.shape[0]
    S = T // nb
    tq = _pick(S, tq)
    nq = S // tq
    tk = tq
    npair = NH // 2

    def body(q_ref, k_ref, v_ref, o_ref, lse_ref, do_ref, dq_ref, dk_ref, dv_ref, lse_t, delta_t):
        lane = lax.broadcasted_iota(jnp.int32, (tq, LANE), 1)
        first = lane < NOPE
        dq_ref[...] = jnp.zeros_like(dq_ref)

        def rows_step(qi, _):
            q0 = pl.multiple_of(qi * tq, tq)
            prod = do_ref[pl.ds(q0, tq), :] * o_ref[pl.ds(q0, tq), :].astype(F32)
            d0 = jnp.sum(jnp.where(first, prod, 0.0), axis=-1, keepdims=True)
            d1 = jnp.sum(jnp.where(first, 0.0, prod), axis=-1, keepdims=True)
            delta_t[:, pl.ds(q0, tq)] = jnp.where(first, d0, d1).T
            lse_t[:, pl.ds(q0, tq)] = lse_ref[pl.ds(q0, tq), :].T
            return 0

        lax.fori_loop(0, nq, rows_step, 0)

        def kv_step(kj, _):
            k0 = pl.multiple_of(kj * tk, tk)
            kbs = [k_ref[pl.ds(k0, tk), hh * LANE:(hh + 1) * LANE] for hh in range(2)]
            vb = v_ref[pl.ds(k0, tk), :]

            def q_block(qi, carry, diagonal):
                dk0, dk1, dv = carry
                dks = [dk0, dk1]
                q0 = pl.multiple_of(qi * tq, tq)
                dov = do_ref[pl.ds(q0, tq), :]
                for hh in range(2):
                    qh = q_ref[pl.ds(q0, tq), hh * LANE:(hh + 1) * LANE]
                    dob = jnp.where(first if hh == 0 else jnp.logical_not(first), dov, 0.0).astype(MX)
                    st = lax.dot_general(kbs[hh], qh, (((1,), (1,)), ((), ())), preferred_element_type=F32)
                    pt = jnp.exp((st - lse_t[hh * NOPE:hh * NOPE + 1, pl.ds(q0, tq)]).astype(MX))
                    if diagonal:
                        pt = jnp.where(_lower_triangle(tq, keys_first=True), pt, jnp.zeros_like(pt))
                    dpt = lax.dot_general(vb, dob, (((1,), (1,)), ((), ())), preferred_element_type=F32)
                    dst = pt * (dpt - delta_t[hh * NOPE:hh * NOPE + 1, pl.ds(q0, tq)]).astype(MX)
                    dks[hh] = dks[hh] + jnp.dot(dst, qh, preferred_element_type=F32)
                    dv = dv + jnp.dot(pt, dob, preferred_element_type=F32)
                    dq_ref[pl.ds(q0, tq), hh * LANE:(hh + 1) * LANE] += lax.dot_general(
                        dst, kbs[hh], (((0,), (0,)), ((), ())), preferred_element_type=F32)
                return dks[0], dks[1], dv

            zero = jnp.zeros((tk, LANE), F32)
            carry = q_block(kj, (zero, zero, zero), True)
            dk0, dk1, dv = lax.fori_loop(kj + 1, nq, lambda qi, c: q_block(qi, c, False), carry)
            dk_ref[pl.ds(k0, tk), 0:LANE] = dk0
            dk_ref[pl.ds(k0, tk), LANE:2 * LANE] = dk1
            dv_ref[pl.ds(k0, tk), :] = dv
            return 0

        lax.fori_loop(0, nq, kv_step, 0)

    pair256 = pl.BlockSpec((S, 2 * LANE), lambda b, p: (b, p))
    pair128 = pl.BlockSpec((S, LANE), lambda b, p: (b, p))
    return _pcall(body, name=name, grid=(nb, npair),
                  in_specs=[pair256, pair256, pair128, pair128, pair128, pair128],
                  out_specs=[pair256, pair256, pair128],
                  out_shape=[_sds((T, NH * LANE), F32), _sds((T, NH * LANE), F32), _sds((T, NH * NOPE), F32)],
                  scratch=[pltpu.VMEM((LANE, S), F32), pltpu.VMEM((LANE, S), F32)], comm=comm)(q, k, v, o, lse, do)


def _run(plan, fn, *args, name, **kw):
    rider = plan.rider(name)
    if rider is None:
        return fn(*args, name=name, **kw)
    outs, landed = fn(*args, name=name, comm=rider, **kw)
    plan.landed(name, landed)
    return outs


def _layer_fwd_bwd(x, tgt, mod3, cos, sin, plan, P):
    nb = mod3.shape[0]
    W = plan.W
    h1, gu1, a1 = _run(plan, _ffn_up_first, x, mod3, P["norm_ffn1"], W["ffn1_in"], sub=0, name="ffn1_up_a")
    gu1, a1 = _run(plan, _ffn_up_second, h1, W["ffn1_in"], gu1, a1, name="ffn1_up_b")
    f1, x1, h2 = _run(plan, _mm_resid_norm, a1, W["ffn1_out"], x, mod3, P["norm_mix"], sub=1, coef=0.5, name="ffn1_out")
    z = _run(plan, _mm, h2, W["w_in"], mode="nt", out_dtype=MX, name="mix_in", tn=1664)
    pooled, mixed, scaled = _pool_fwd(z, W["pool_grp"], P["pool_scale"], nb=nb, name="pool_fwd")
    br_pool = _mm(scaled, W["pool_proj"], mode="nn", out_dtype=MX, name="pool_proj")
    qn, kvn, qp, kvp, q, k, v = _mla_in_fwd(z, W["q_up"], W["kv_up"], P["q_a_norm"], P["kv_a_norm"], cos, sin,
                                            P["q_gain"], P["k_gain"], name="mla_in_fwd")
    attn, lse = _run(plan, _attn_fwd, q, k, v, nb=nb, name="attn_fwd")
    br_mla, merged, mo, x2, h3 = _mix_out_fwd(z, br_pool, attn, W["mla_proj"], W["w_out"], x1, mod3, P["norm_ffn2"], sub=2,
                                              name="mix_out", tm=256)
    gu2, a2 = _ffn_up(h3, W["ffn2_in"], name="ffn2_up")
    dy, df2, st_fin, loss = _mm_loss(a2, W["ffn2_out"], x2, tgt, mod3, name="ffn2_out_loss")

    plan.grad("ffn2_out", _dw(a2, df2, name="d_ffn2_out"))
    dgu2 = _ffn_dact(df2, gu2, W["ffn2_out"], name="ffn2_dact")
    plan.grad("ffn2_in", _ffn_dw_in(h3, dgu2, name="d_ffn2_in"))
    dx2, dmo, st3 = _run(plan, _ffn_dh_norm, dgu2, W["ffn2_in"], x2, dy, mo, mod3, P["norm_ffn2"], sub=2, coef=1.0,
                         name="d_ffn2_h")
    plan.grad("w_out", _dw(merged, dmo, name="d_mix_out"))
    dbr_pool, dbr_mla, dgates, dattn = _mix_out_bwd(dmo, W["w_out"], z, br_pool, br_mla, W["mla_proj"], name="mix_out_bwd", tm=256)
    plan.grad("pool_proj", _dw(scaled, dbr_pool, name="d_pool_proj"))
    dscaled = _mm(dbr_pool, W["pool_proj"], mode="nt", out_dtype=F32, name="d_pool_scaled")
    du_pool, d_pool_grp, d_pool_scale = _pool_bwd(dscaled, mixed, pooled, W["pool_grp"], P["pool_scale"], nb=nb, name="pool_bwd")
    plan.grad("mla_proj", _dw(attn, dbr_mla, name="d_mla_proj"))
    dq, dk, dv = _run(plan, _attn_bwd, q, k, v, attn, lse, dattn, nb=nb, name="attn_bwd")
    dqp, dkvp, dkr, dql, dkvl, st_prep, st_q, st_kv = _mla_in_bwd(
        dq, dk, dv, qp, kvp, z, W["q_up"], W["kv_up"], P["q_a_norm"], P["kv_a_norm"], cos, sin, P["q_gain"], P["k_gain"],
        nb=nb, name="mla_in_bwd")
    plan.grad("q_up", _dw(qn, dqp, name="d_q_up"))
    plan.grad("kv_up", _dw(kvn, dkvp, name="d_kv_up"))
    dz = jnp.concatenate([du_pool, dql, dkvl, dkr, dgates], axis=1)
    plan.grad("w_in", _run(plan, _dw, h2, dz, name="d_mix_in", tm=256, tn=1664, out_t=True))
    dx1, df1, st2 = _run(plan, _mm_norm_bwd, dz, W["w_in"], x1, dx2, f1, mod3, P["norm_mix"], sub=1, coef=0.5, name="d_mix_h")
    plan.grad("ffn1_out", _run(plan, _dw, a1, df1, name="d_ffn1_out"))
    dgu1 = _run(plan, _ffn_dact, df1, gu1, W["ffn1_out"], name="ffn1_dact")
    half = x.shape[1] // 2
    plan.grad("ffn1_in@0", _run(plan, _ffn_dw_in, h1, dgu1, rows=(0, half), name="d_ffn1_in_a"))
    plan.grad("ffn1_in@1", _run(plan, _ffn_dw_in, h1, dgu1, rows=(half, half), name="d_ffn1_in_b"))
    dh1 = _run(plan, _ffn_dh, dgu1, W["ffn1_in"], name="d_ffn1_h")
    grad_x, st1 = _run(plan, _bwd_block, x, dh1, dx1, None, mod3, P["norm_ffn1"], sub=0, coef=0.0, name="bwd_norm1")

    return loss, grad_x, (st1, st2, st3, st_fin, st_q, st_kv, st_prep, d_pool_scale), d_pool_grp


def _padded(rows):
    return rows + -rows % (4 * SUBLANE)


def _pad_rows(a):
    pad = [(0, 0)] * a.ndim
    pad[-2] = (0, _padded(a.shape[-2]) - a.shape[-2])
    return jnp.pad(a, pad)


def _w_in_pieces(shard):
    runs = []
    for ref0, ref1, k0 in ((0, Z_KR, 0), (Z_KR, Z_KR + ROPE, Z_KR + KR_LANE), (Z_KR + ROPE, N_CHIP * shard, Z_GP)):
        for q in range(N_CHIP):
            lo, hi = max(ref0, q * shard), min(ref1, (q + 1) * shard)
            if lo < hi:
                runs.append((k0 + lo - ref0, q * _padded(shard) + lo - q * shard, hi - lo))
    return runs


def _w_in_stacked_to_kernel(st, shard):
    flat = st.reshape(-1, st.shape[2])
    parts, at = [], 0
    for k0, s0, n in sorted(_w_in_pieces(shard)):
        parts += [jnp.zeros((k0 - at, flat.shape[1]), st.dtype)] * (k0 > at) + [flat[s0:s0 + n]]
        at = k0 + n
    assert at == Z_W
    return jnp.concatenate(parts, axis=0)


def _w_in_kernel_to_stacked(gt, shard):
    parts, at = [], 0
    for s0, k0, n in sorted((s, k, n) for k, s, n in _w_in_pieces(shard)) + [(N_CHIP * _padded(shard), 0, 0)]:
        parts += [jnp.zeros((s0 - at, gt.shape[1]), gt.dtype)] * (s0 > at) + [gt[k0:k0 + n]] * (n > 0)
        at = s0 + n
    return jnp.concatenate(parts, axis=0).reshape(N_CHIP, _padded(shard), gt.shape[1])


def _q_up_to_kernel(w):
    k = w.shape[0]
    return jnp.pad(w.reshape(k, NH, NOPE + ROPE), ((0, 0), (0, 0), (0, LANE - NOPE - ROPE))).reshape(k, NH * LANE)


def _q_up_from_kernel(g):
    k = g.shape[0]
    return g.reshape(k, NH, LANE)[:, :, :NOPE + ROPE].reshape(k, NH * (NOPE + ROPE))


def _kv_up_to_kernel(w):
    k = w.shape[0]
    w3 = w.reshape(k, NH, 2 * NOPE)
    kpart = jnp.pad(w3[:, :, :NOPE], ((0, 0), (0, 0), (0, LANE - NOPE))).reshape(k, NH * LANE)
    return jnp.concatenate([kpart, w3[:, :, NOPE:].reshape(k, NH * NOPE)], axis=1)


def _kv_up_from_kernel(g):
    k = g.shape[0]
    kpart = g[:, :NH * LANE].reshape(k, NH, LANE)[:, :, :NOPE]
    vpart = g[:, NH * LANE:].reshape(k, NH, NOPE)
    return jnp.concatenate([kpart, vpart], axis=2).reshape(k, NH * 2 * NOPE)


def _gain_slab(nope, rope):
    return jnp.concatenate([nope, rope, jnp.zeros((1, LANE - NOPE - ROPE), nope.dtype)], axis=1)


def _rope_tables(positions):
    inv_freq = 10000.0 ** (-jnp.arange(0, ROPE, 2, dtype=F32) / ROPE)
    ang = positions.astype(F32)[:, None] * inv_freq
    ang = jnp.concatenate([ang, ang], axis=-1)
    t = positions.shape[0]
    cos = jnp.concatenate([jnp.ones((t, KR_LANE), F32), jnp.cos(ang), jnp.ones((t, LANE - KR_LANE - ROPE), F32)], axis=1)
    sin = jnp.concatenate([jnp.zeros((t, KR_LANE), F32), jnp.sin(ang), jnp.zeros((t, LANE - KR_LANE - ROPE), F32)], axis=1)
    return cos, sin


def _coords():
    return lax.axis_index("x"), lax.axis_index("y"), lax.axis_index("c")


HBM_SPEC = pl.BlockSpec(memory_space=pl.ANY)
VMEM_SPEC = pl.BlockSpec(memory_space=pltpu.VMEM)


CHIP_RELS = ((1, 0), (0, 1), (1, 1))


def _gather_comm(shards):
    n = len(shards)

    def ici(ins, outs, sems, a, j, x, y, cc):
        half = ins[a].shape[0] // 2
        mine = pl.ds(cc * half, half)
        dx, dy = CHIP_RELS[j]
        return pltpu.make_async_remote_copy(src_ref=ins[a].at[mine], dst_ref=outs[a].at[2 * x + y, mine],
                                            send_sem=sems[0].at[a, j], recv_sem=sems[1].at[a, j],
                                            device_id=(x ^ dx, y ^ dy, cc), device_id_type=MESH)

    def d2d(ins, outs, sems, a, j, x, y, cc, half_of):
        half = ins[a].shape[0] // 2
        dx, dy = CHIP_RELS[j]
        landed = outs[a].at[2 * (x ^ dx) + (y ^ dy), pl.ds(half_of * half, half)]
        return pltpu.make_async_remote_copy(src_ref=landed, dst_ref=landed, send_sem=sems[2].at[a, j], recv_sem=sems[3].at[a, j],
                                            device_id=(x, y, 1 - cc), device_id_type=MESH)

    def own(ins, outs, sems, a, x, y):
        return pltpu.make_async_copy(ins[a], outs[a].at[2 * x + y], sems[4].at[a])

    def start(ins, outs, sems):
        x, y, cc = _coords()
        for a in range(n):
            own(ins, outs, sems, a, x, y).start()
            for j in range(3):
                ici(ins, outs, sems, a, j, x, y, cc).start()

    def finish(ins, outs, sems):
        x, y, cc = _coords()
        for a in range(n):
            for j in range(3):
                ici(ins, outs, sems, a, j, x, y, cc).wait_recv()
                d2d(ins, outs, sems, a, j, x, y, cc, cc).start()
        for a in range(n):
            for j in range(3):
                d2d(ins, outs, sems, a, j, x, y, cc, 1 - cc).wait_recv()
        for a in range(n):
            for j in range(3):
                ici(ins, outs, sems, a, j, x, y, cc).wait_send()
                d2d(ins, outs, sems, a, j, x, y, cc, cc).wait_send()
            own(ins, outs, sems, a, x, y).wait()

    dma = pltpu.SemaphoreType.DMA
    return _Comm(shards, [_sds((N_CHIP,) + s.shape, s.dtype) for s in shards],
                 [dma((n, 3)), dma((n, 3)), dma((n, 3)), dma((n, 3)), dma((n,))], start, finish)


def _swap_comm(parts):
    n = len(parts)

    def copy(ins, outs, sems, a):
        x, y, cc = _coords()
        half = ins[a].shape[1] // 2
        return pltpu.make_async_remote_copy(src_ref=ins[a].at[:, pl.ds((1 - cc) * half, half)], dst_ref=outs[a],
                                            send_sem=sems[0].at[a], recv_sem=sems[1].at[a], device_id=(x, y, 1 - cc),
                                            device_id_type=MESH)

    def start(ins, outs, sems):
        for a in range(n):
            copy(ins, outs, sems, a).start()

    def finish(ins, outs, sems):
        for a in range(n):
            copy(ins, outs, sems, a).wait()

    dma = pltpu.SemaphoreType.DMA
    return _Comm(parts, [_sds((p.shape[0], p.shape[1] // 2, p.shape[2]), p.dtype) for p in parts], [dma((n,)), dma((n,))],
                 start, finish)


def _add_half(full, other, cidx, *, name):
    nch, r, c = full.shape
    half = r // 2
    tr = _pick_rows(half)
    nbk = half // tr
    grid_spec = pltpu.PrefetchScalarGridSpec(
        num_scalar_prefetch=1, grid=(nch, nbk),
        in_specs=[pl.BlockSpec((1, tr, c), lambda j, i, cref: (j, cref[0] * nbk + i, 0)),
                  pl.BlockSpec((1, tr, c), lambda j, i, cref: (j, i, 0))],
        out_specs=pl.BlockSpec((1, tr, c), lambda j, i, cref: (j, i, 0)))

    def body(cref, a_ref, b_ref, o_ref):
        o_ref[...] = (a_ref[...] + b_ref[...]).astype(o_ref.dtype)

    return _pcall(body, name=name, out_shape=_sds((nch, half, c), MX), grid_spec=grid_spec)(cidx, full, other)


def _pick_rows(rows, target=512):
    best = None
    for t in range(16, min(rows, target) + 1, 16):
        if rows % t == 0:
            best = t
    return rows if best is None else best


def _exchange_comm(parts):
    n = len(parts)

    def send(ins, outs, sems, a, j, x, y, cc):
        dx, dy = CHIP_RELS[j]
        return pltpu.make_async_remote_copy(src_ref=ins[a].at[2 * (x ^ dx) + (y ^ dy)], dst_ref=outs[a].at[2 * x + y],
                                            send_sem=sems[0].at[a, j], recv_sem=sems[1].at[a, j],
                                            device_id=(x ^ dx, y ^ dy, cc), device_id_type=MESH)

    def landing(ins, outs, sems, a, j, x, y, cc):
        dx, dy = CHIP_RELS[j]
        peer_chip = 2 * (x ^ dx) + (y ^ dy)
        return pltpu.make_async_remote_copy(src_ref=ins[a].at[peer_chip], dst_ref=outs[a].at[peer_chip], send_sem=sems[0].at[a, j],
                                            recv_sem=sems[1].at[a, j], device_id=(x, y, cc), device_id_type=MESH)

    def own(ins, outs, sems, a, x, y):
        return pltpu.make_async_copy(ins[a].at[2 * x + y], outs[a].at[2 * x + y], sems[2].at[a])

    def start(ins, outs, sems):
        x, y, cc = _coords()
        for a in range(n):
            own(ins, outs, sems, a, x, y).start()
            for j in range(3):
                send(ins, outs, sems, a, j, x, y, cc).start()

    def finish(ins, outs, sems):
        x, y, cc = _coords()
        for a in range(n):
            for j in range(3):
                landing(ins, outs, sems, a, j, x, y, cc).wait_recv()
        for a in range(n):
            for j in range(3):
                send(ins, outs, sems, a, j, x, y, cc).wait_send()
            own(ins, outs, sems, a, x, y).wait()

    dma = pltpu.SemaphoreType.DMA
    return _Comm(parts, [_sds(p.shape, p.dtype) for p in parts], [dma((n, 3)), dma((n, 3)), dma((n,))], start, finish)


def _sum_chips(q, cidx, *, name):
    nch, h, c = q.shape
    tr = _pick_rows(h)
    nbk = h // tr
    grid_spec = pltpu.PrefetchScalarGridSpec(
        num_scalar_prefetch=1, grid=(nbk,),
        in_specs=[pl.BlockSpec((nch, tr, c), lambda i, cref: (0, i, 0))],
        out_specs=pl.BlockSpec((tr, c), lambda i, cref: (cref[0] * nbk + i, 0)))

    def body(cref, q_ref, o_ref):
        acc = q_ref[0].astype(F32) + q_ref[1].astype(F32)
        acc = acc + q_ref[2].astype(F32)
        o_ref[...] = acc + q_ref[3].astype(F32)

    return _pcall(body, name=name, out_shape=_sds((2 * h, c), F32), grid_spec=grid_spec)(cidx, q)


def _join_comm(fulls):
    n = len(fulls)

    def half_copy(outs, sems, a, which):
        x, y, cc = _coords()
        h = outs[a].shape[0] // 2
        rows = outs[a].at[pl.ds((cc if which == 0 else 1 - cc) * h, h)]
        return pltpu.make_async_remote_copy(src_ref=rows, dst_ref=rows, send_sem=sems[0].at[a], recv_sem=sems[1].at[a],
                                            device_id=(x, y, 1 - cc), device_id_type=MESH)

    def start(ins, outs, sems):
        for a in range(n):
            half_copy(outs, sems, a, 0).start()

    def finish(ins, outs, sems):
        for a in range(n):
            half_copy(outs, sems, a, 1).wait_recv()
        for a in range(n):
            half_copy(outs, sems, a, 0).wait_send()

    dma = pltpu.SemaphoreType.DMA
    return _Comm(fulls, [_sds(p.shape, p.dtype) for p in fulls], [dma((n,)), dma((n,))], start, finish,
                 aliases={a: a for a in range(n)})


def _ada_prologue(c, w, b, *, name, comm=None):
    nb, dm = c.shape
    n = w.shape[1]

    def body(c_ref, w_ref, b_ref, call_ref, land_ref, cpad, mod_s, g_send, g_recv, m_send, m_recv):
        x, y, cc = _coords()
        me = 4 * x + 2 * y + cc
        chip = 2 * x + y
        cpad[...] = jnp.zeros_like(cpad)
        cpad[0:nb, :] = c_ref[...]
        copies = []
        for kk in range(1, N_DEV):
            peer = (x ^ (kk >> 2), y ^ ((kk >> 1) & 1), cc ^ (kk & 1))
            cp = pltpu.make_async_remote_copy(src_ref=cpad, dst_ref=call_ref.at[me], send_sem=g_send.at[kk - 1],
                                              recv_sem=g_recv.at[kk - 1], device_id=peer, device_id_type=MESH)
            cp.start()
            copies.append(cp)
        call_ref[me] = cpad[...]
        for kk in range(1, N_DEV):
            pltpu.make_async_remote_copy(src_ref=cpad, dst_ref=call_ref.at[me ^ kk], send_sem=g_send.at[kk - 1],
                                         recv_sem=g_recv.at[kk - 1], device_id=(x, y, cc), device_id_type=MESH).wait_recv()
        cv = call_ref[...].reshape(N_DEV * SUBLANE, dm)
        act = (cv * _sigmoid(cv)).astype(MX)
        mod = jnp.dot(act, w_ref[...].astype(MX), preferred_element_type=F32) + b_ref[...]
        mod_s[...] = mod.reshape(N_DEV, SUBLANE, n)
        for j, (dx, dy) in enumerate(CHIP_RELS):
            cp = pltpu.make_async_remote_copy(src_ref=mod_s.at[4 * (x ^ dx) + 2 * (y ^ dy) + cc], dst_ref=land_ref.at[chip],
                                              send_sem=m_send.at[j], recv_sem=m_recv.at[j],
                                              device_id=(x ^ dx, y ^ dy, cc), device_id_type=MESH)
            cp.start()
            copies.append(cp)
        land_ref[chip] = mod_s[me]
        for j, (dx, dy) in enumerate(CHIP_RELS):
            pltpu.make_async_remote_copy(src_ref=mod_s.at[me], dst_ref=land_ref.at[2 * (x ^ dx) + (y ^ dy)], send_sem=m_send.at[j],
                                         recv_sem=m_recv.at[j], device_id=(x, y, cc), device_id_type=MESH).wait_recv()
        for cp in copies:
            cp.wait_send()

    dma = pltpu.SemaphoreType.DMA
    return _pcall(body, name=name, in_specs=[VMEM_SPEC] * 3, out_specs=[VMEM_SPEC] * 2,
                  out_shape=[_sds((N_DEV, SUBLANE, dm), F32), _sds((N_CHIP, SUBLANE, n), F32)],
                  scratch=[pltpu.VMEM((SUBLANE, dm), F32), pltpu.VMEM((N_DEV, SUBLANE, n), F32),
                           dma((N_DEV - 1,)), dma((N_DEV - 1,)), dma((3,)), dma((3,))], comm=comm)(c, w, b)


def _ada_bwd(c_all, dmod_cols, *, name):
    m, kdim = c_all.shape
    n = dmod_cols.shape[1]
    tn = _pick(n, 1152)

    def body(c_ref, d_ref, o_ref):
        cv = c_ref[...]
        act = (cv * _sigmoid(cv)).astype(MX)
        o_ref[...] = lax.dot_general(act, d_ref[...].astype(MX), (((0,), (0,)), ((), ())), preferred_element_type=F32)

    return _pcall(body, name=name, out_shape=_sds((kdim, n), F32), grid=(n // tn,),
                  in_specs=[pl.BlockSpec((m, kdim), lambda j: (0, 0)), pl.BlockSpec((m, tn), lambda j: (0, j))],
                  out_specs=pl.BlockSpec((kdim, tn), lambda j: (0, j)))(c_all, dmod_cols)


SLAB_W = 1024
RED_ROWS = 8


LOSS_LANE = SLAB_W - LANE


def _pack_stats(st1, st2, st3, st_fin, st_q, st_kv, st_prep, d_pool_scale, loss8, *, name):
    nb = st1.shape[0]
    dm_rows = -(-9 * nb // SUBLANE) * SUBLANE

    def body(s1, s2, s3, sf, sq, skv, sp, sps, loss_ref, o_ref):
        o_ref[...] = jnp.zeros_like(o_ref)
        for s in range(nb):
            rows = [s1[s, 0:1, :], s1[s, 1:2, :], s2[s, 3:4, :], s2[s, 0:1, :], s2[s, 1:2, :], s3[s, 3:4, :], s3[s, 0:1, :],
                    s3[s, 1:2, :], sf[s, 0:1, :]]
            for k, row in enumerate(rows):
                o_ref[9 * s + k:9 * s + k + 1, :] = row

        def over_seq(ref, r):
            acc = ref[0, r:r + 1, :]
            for s in range(1, nb):
                acc = acc + ref[s, r:r + 1, :]
            return acc

        o_ref[dm_rows + 0:dm_rows + 1, :] = over_seq(s1, 2)
        o_ref[dm_rows + 1:dm_rows + 2, :] = over_seq(s2, 2)
        o_ref[dm_rows + 2:dm_rows + 3, :] = over_seq(s3, 2)
        o_ref[dm_rows + 3:dm_rows + 4, 0:POOL_W] = over_seq(sps, 0)
        o_ref[dm_rows + 4:dm_rows + 5, 0:QL] = over_seq(sq, 0)
        o_ref[dm_rows + 5:dm_rows + 6, 0:KVL] = over_seq(skv, 0)
        o_ref[dm_rows + 6:dm_rows + 7, 0:LANE] = over_seq(sp, 0)
        o_ref[dm_rows + 7:dm_rows + 8, 0:LANE] = over_seq(sp, 1)
        o_ref[dm_rows + 7:dm_rows + 8, LOSS_LANE:] = loss_ref[0:1, :]

    return _pcall(body, name=name, out_shape=_sds((dm_rows + RED_ROWS, SLAB_W), F32), in_specs=[VMEM_SPEC] * 9,
                  out_specs=VMEM_SPEC)(st1, st2, st3, st_fin, st_q, st_kv, st_prep, d_pool_scale, loss8)


def _small_allreduce(slab, pool, *, name, comm=None):
    rows, w = slab.shape
    dm = rows - RED_ROWS
    prow, pw = pool.shape
    crow = RED_ROWS + 2 * dm

    def body(slab_ref, pool_ref, red_ref, dmod_ref, ptot_ref, sib_slab, sib_pool, chip_slab, chip_pool, land_slab, land_pool,
             a_send, a_recv, b_send, b_recv):
        x, y, cc = _coords()
        chip = 2 * x + y
        sib = (x, y, 1 - cc)
        to_sib = [pltpu.make_async_remote_copy(src_ref=slab_ref, dst_ref=sib_slab, send_sem=a_send.at[0], recv_sem=a_recv.at[0],
                                               device_id=sib, device_id_type=MESH),
                  pltpu.make_async_remote_copy(src_ref=pool_ref, dst_ref=sib_pool, send_sem=a_send.at[1], recv_sem=a_recv.at[1],
                                               device_id=sib, device_id_type=MESH)]
        for cp in to_sib:
            cp.start()
        for cp in to_sib:
            cp.wait()
        mine_dm, theirs_dm = slab_ref[0:dm, :], sib_slab[0:dm, :]
        chip_slab[0:RED_ROWS, :] = slab_ref[dm:, :] + sib_slab[dm:, :]
        chip_slab[RED_ROWS:RED_ROWS + dm, :] = jnp.where(cc == 0, mine_dm, theirs_dm)
        chip_slab[RED_ROWS + dm:, :] = jnp.where(cc == 0, theirs_dm, mine_dm)
        chip_pool[...] = pool_ref[...] + sib_pool[...]

        sends = []
        for j, (dx, dy) in enumerate(CHIP_RELS):
            peer = (x ^ dx, y ^ dy, cc)
            sends.append(pltpu.make_async_remote_copy(src_ref=chip_slab, dst_ref=land_slab.at[chip], send_sem=b_send.at[j, 0],
                                                      recv_sem=b_recv.at[j, 0], device_id=peer, device_id_type=MESH))
            sends.append(pltpu.make_async_remote_copy(src_ref=chip_pool, dst_ref=land_pool.at[chip], send_sem=b_send.at[j, 1],
                                                      recv_sem=b_recv.at[j, 1], device_id=peer, device_id_type=MESH))
        for cp in sends:
            cp.start()
        land_slab[chip] = chip_slab[...]
        land_pool[chip] = chip_pool[...]
        for j, (dx, dy) in enumerate(CHIP_RELS):
            peer_chip = 2 * (x ^ dx) + (y ^ dy)
            pltpu.make_async_remote_copy(src_ref=chip_slab, dst_ref=land_slab.at[peer_chip], send_sem=b_send.at[j, 0],
                                         recv_sem=b_recv.at[j, 0], device_id=(x, y, cc), device_id_type=MESH).wait_recv()
            pltpu.make_async_remote_copy(src_ref=chip_pool, dst_ref=land_pool.at[peer_chip], send_sem=b_send.at[j, 1],
                                         recv_sem=b_recv.at[j, 1], device_id=(x, y, cc), device_id_type=MESH).wait_recv()
        red = land_slab[0, 0:RED_ROWS, :]
        ptot = land_pool[0]
        for ch in range(1, N_CHIP):
            red = red + land_slab[ch, 0:RED_ROWS, :]
            ptot = ptot + land_pool[ch]
        red_ref[...] = red
        ptot_ref[...] = ptot
        for ch in range(N_CHIP):
            dmod_ref[2 * dm * ch:2 * dm * (ch + 1), :] = land_slab[ch, RED_ROWS:, :]
        for cp in sends:
            cp.wait_send()

    dma = pltpu.SemaphoreType.DMA
    return _pcall(body, name=name, in_specs=[VMEM_SPEC, VMEM_SPEC], out_specs=[VMEM_SPEC] * 3,
                  out_shape=[_sds((RED_ROWS, w), F32), _sds((N_DEV * dm, w), F32), _sds((prow, pw), F32)],
                  scratch=[pltpu.VMEM((rows, w), F32), pltpu.VMEM((prow, pw), F32), pltpu.VMEM((crow, w), F32),
                           pltpu.VMEM((prow, pw), F32), pltpu.VMEM((N_CHIP, crow, w), F32), pltpu.VMEM((N_CHIP, prow, pw), F32),
                           dma((2,)), dma((2,)), dma((3, 2)), dma((3, 2))], comm=comm)(slab, pool)


def _adamw_math(w, g, m, v):
    mn = ADAM_B1 * m + (1.0 - ADAM_B1) * g
    vn = ADAM_B2 * v + (1.0 - ADAM_B2) * (g * g)
    bc1 = 1.0 / (1.0 - ADAM_B1 ** ADAM_STEP)
    bc2 = 1.0 / (1.0 - ADAM_B2 ** ADAM_STEP)
    return -ADAM_LR * ((mn * bc1) / (jnp.sqrt(vn * bc2) + ADAM_EPS) + ADAM_WD * w), mn, vn


def _small_update(red, dmod_all, pool_total, nb, params, *, name):
    names = list(SMALL)
    dm = dmod_all.shape[0] // N_DEV

    def grad_of(nm, red_ref, dmod_ref, ptot_ref):
        if nm == "b_ada":
            acc = None
            for d in range(N_DEV):
                for s in range(nb):
                    blk = dmod_ref[d * dm + 9 * s:d * dm + 9 * s + 9, :]
                    acc = blk if acc is None else acc + blk
            return jnp.concatenate([acc[k:k + 1, :] for k in range(9)], axis=1)
        if nm == "pool_grp":
            return ptot_ref[...]
        row, lo, n = {"norm_ffn1": (0, 0, D), "norm_mix": (1, 0, D), "norm_ffn2": (2, 0, D), "pool_scale": (3, 0, POOL_W),
                      "q_a_norm": (4, 0, QL), "kv_a_norm": (5, 0, KVL), "q_norm_nope": (6, 0, NOPE),
                      "q_norm_rope": (6, NOPE, ROPE), "k_norm_nope": (7, 0, NOPE), "k_norm_rope": (7, KR_LANE, ROPE)}[nm]
        return red_ref[row:row + 1, lo:lo + n]

    def body(*refs):
        red_ref, dmod_ref, ptot_ref = refs[:3]
        ins = refs[3:3 + 3 * len(names)]
        outs = refs[3 + 3 * len(names):]
        outs[4 * len(names)][...] = red_ref[RED_ROWS - 1:RED_ROWS, LOSS_LANE:]
        for i, nm in enumerate(names):
            g = grad_of(nm, red_ref, dmod_ref, ptot_ref)
            d, mn, vn = _adamw_math(ins[3 * i][...], g, ins[3 * i + 1][...], ins[3 * i + 2][...])
            outs[4 * i][...] = g
            outs[4 * i + 1][...] = d
            outs[4 * i + 2][...] = mn
            outs[4 * i + 3][...] = vn

    flat_in = [a for nm in names for a in params[nm]]
    out_shape = [_sds(params[nm][0].shape, F32) for nm in names for _ in range(4)] + [_sds((1, LANE), F32)]
    res = _pcall(body, name=name, in_specs=[VMEM_SPEC] * (3 + len(flat_in)), out_specs=[VMEM_SPEC] * len(out_shape),
                 out_shape=out_shape)(red, dmod_all, pool_total, *flat_in)
    return {nm: tuple(res[4 * i:4 * i + 4]) for i, nm in enumerate(names)}, res[-1]


def _adamw(w, g, m, v, *, name, comm=None):
    r, c = w.shape
    tr = _pick_rows(r, 256)
    tc = c if tr < r else _pick(c, 256)
    spec = pl.BlockSpec((tr, tc), lambda i, j: (i, j))
    bc1 = 1.0 / (1.0 - ADAM_B1 ** ADAM_STEP)
    bc2 = 1.0 / (1.0 - ADAM_B2 ** ADAM_STEP)

    def body(w_ref, g_ref, m_ref, v_ref, d_ref, mo_ref, vo_ref):
        gv = g_ref[...]
        mn = ADAM_B1 * m_ref[...] + (1.0 - ADAM_B1) * gv
        vn = ADAM_B2 * v_ref[...] + (1.0 - ADAM_B2) * (gv * gv)
        mo_ref[...] = mn
        vo_ref[...] = vn
        d_ref[...] = -ADAM_LR * ((mn * bc1) / (jnp.sqrt(vn * bc2) + ADAM_EPS) + ADAM_WD * w_ref[...])

    out = _sds((r, c), F32)
    return _pcall(body, name=name, out_shape=[out, out, out], grid=(r // tr, c // tc), in_specs=[spec] * 4, out_specs=[spec] * 3,
                  comm=comm)(w, g, m, v)


BIG = ("w_ffn1_in", "w_ffn1_out", "w_in", "w_pool_proj", "w_q_up", "w_kv_up", "w_mla_proj", "w_out", "w_ffn2_in", "w_ffn2_out")
ROW_SHARDED = ("w_ffn1_out", "w_out", "w_ffn2_out")
KERNEL_NAME = {"w_ffn1_in": "ffn1_in", "w_ffn1_out": "ffn1_out", "w_in": "w_in", "w_pool_proj": "pool_proj", "w_q_up": "q_up",
               "w_kv_up": "kv_up", "w_mla_proj": "mla_proj", "w_out": "w_out", "w_ffn2_in": "ffn2_in", "w_ffn2_out": "ffn2_out"}
WEIGHTS = ("w_ada", "b_ada", "norm_ffn1", "w_ffn1_in", "w_ffn1_out", "norm_mix", "w_in", "pool_grp", "pool_scale", "w_pool_proj",
           "q_a_norm", "w_q_up", "kv_a_norm", "w_kv_up", "q_norm_nope", "q_norm_rope", "k_norm_nope", "k_norm_rope", "w_mla_proj",
           "w_out", "norm_ffn2", "w_ffn2_in", "w_ffn2_out")
SMALL = ("b_ada", "norm_ffn1", "norm_mix", "pool_grp", "pool_scale", "q_a_norm", "kv_a_norm", "q_norm_nope", "q_norm_rope",
         "k_norm_nope", "k_norm_rope", "norm_ffn2")


def _assemble(name, stacked):
    if name in ROW_SHARDED:
        return stacked.reshape(stacked.shape[0] * stacked.shape[1], stacked.shape[2])
    return jnp.transpose(stacked, (1, 0, 2)).reshape(stacked.shape[1], stacked.shape[0] * stacked.shape[2])


def _split(name, full):
    if name in ROW_SHARDED:
        return full.reshape(N_CHIP, full.shape[0] // N_CHIP, full.shape[1])
    return jnp.transpose(full.reshape(full.shape[0], N_CHIP, full.shape[1] // N_CHIP), (1, 0, 2))


GU = ("w_ffn1_in", "w_ffn2_in")
NARROW = ("w_in", "w_q_up")
TRANSPOSED = ("w_in",)
W_IN_SHARD = (Z_W - LANE + ROPE) // N_CHIP


MIX_SMALL = ("w_out", "w_pool_proj", "w_mla_proj", "w_q_up", "w_kv_up")
RIDES = {
    "ada_prologue": (("gather", ("w_ffn1_in",)),),
    "ffn1_up_a": (("gather", ("w_ffn1_out",)),),
    "ffn1_up_b": (("gather", ("w_in",)),),
    "mix_in": (("gather", MIX_SMALL),),
    "attn_fwd": (("gather", ("w_ffn2_in", "w_ffn2_out")),),
    "d_ffn2_h": (("swap", ("w_ffn2_out", "w_ffn2_in")),),
    "attn_bwd": (("exchange", ("w_ffn2_out", "w_ffn2_in")),),
    "d_mix_in": (("swap", MIX_SMALL),),
    "d_mix_h": (("exchange", MIX_SMALL), ("swap", ("w_in",))),
    "ffn1_dact": (("exchange", ("w_in",)), ("swap", ("w_ffn1_out",))),
    "d_ffn1_in_a": (("exchange", ("w_ffn1_out",)),),
    "d_ffn1_in_b": (("swap", ("w_ffn1_in@0",)),),
    "d_ffn1_h": (("exchange", ("w_ffn1_in@0",)), ("swap", ("w_ffn1_in@1",))),
    "bwd_norm1": (("exchange", ("w_ffn1_in@1",)),),
    "small_allreduce": (("join", tuple(n for n in BIG if n != "w_ffn1_in") + ("w_ffn1_in@0", "w_ffn1_in@1")),),
}


def _kname(n):
    base, _, part = n.partition("@")
    return KERNEL_NAME[base] + ("@" + part if part else "")


def _both(comms):
    if len(comms) == 1:
        return comms[0]
    ins, outs, sems, aliases, spans = [], [], [], {}, []
    for c in comms:
        spans.append((len(ins), len(c.ins), len(outs), len(c.out_shapes), len(sems), len(c.sems)))
        aliases.update({len(ins) + i: len(outs) + o for i, o in c.aliases.items()})
        ins, outs, sems = ins + c.ins, outs + c.out_shapes, sems + c.sems

    def each(which):
        def run(i_, o_, s_):
            for c, (ia, ni, oa, no, sa, ns) in zip(comms, spans):
                getattr(c, which)(i_[ia:ia + ni], o_[oa:oa + no], s_[sa:sa + ns])
        return run

    return _Comm(ins, outs, sems, each("start"), each("finish"), aliases)


class _ExchangePlan:
    def __init__(self, shards, cidx):
        self.shards, self.cidx = shards, cidx
        self.W, self.G, self.parts, self.pre, self.reduced, self.joined = {}, {}, {}, {}, {}, {}

    def grad(self, key, g):
        self.G[key] = g

    def rider(self, name):
        comms = []
        for kind, names in RIDES.get(name, ()):
            if kind == "gather":
                comms.append(_gather_comm([self.shards[n] for n in names]))
            elif kind == "swap":
                for n in names:
                    self.parts[n] = self._stacked(n)
                comms.append(_swap_comm([self.parts[n] for n in names]))
            elif kind == "exchange":
                comms.append(_exchange_comm([self.pre[n] for n in names]))
            else:
                comms.append(_join_comm([self.reduced[n] for n in names]))
        return _both(comms) if comms else None

    def landed(self, name, outs):
        at = 0
        for kind, names in RIDES[name]:
            for n, o in zip(names, outs[at:at + len(names)]):
                if kind == "gather":
                    self.W[KERNEL_NAME[n]] = self._to_kernel(n, o)
                elif kind == "swap":
                    self.pre[n] = _add_half(self.parts[n], o, self.cidx, name="rs_add_" + _kname(n))
                elif kind == "exchange":
                    self.reduced[n] = _sum_chips(o, self.cidx, name="rs_sum_" + _kname(n))
                else:
                    self.joined[n] = o
            at += len(names)

    @staticmethod
    def _to_kernel(n, stacked):
        if n in GU:
            return stacked
        if n in TRANSPOSED:
            return _w_in_stacked_to_kernel(stacked, W_IN_SHARD)
        full = _assemble(n, stacked)
        return {"w_q_up": _q_up_to_kernel, "w_kv_up": _kv_up_to_kernel}.get(n, lambda w: w)(full)

    def _stacked(self, n):
        g = self.G[_kname(n)]
        if n.partition("@")[0] in GU:
            return g
        if n in TRANSPOSED:
            return _w_in_kernel_to_stacked(g, W_IN_SHARD)
        full = {"w_q_up": _q_up_from_kernel, "w_kv_up": _kv_up_from_kernel}.get(n, lambda w: w)(g)
        return _split(n, full)


def kernel(x, c, positions, w_ada, b_ada, norm_ffn1, w_ffn1_in, w_ffn1_out, norm_mix, w_in, pool_grp, pool_scale, w_pool_proj, q_a_norm, w_q_up, kv_a_norm, w_kv_up, q_norm_nope, q_norm_rope, k_norm_nope, k_norm_rope, w_mla_proj, w_out, norm_ffn2, w_ffn2_in, w_ffn2_out, loss_target, m_w_ada, m_b_ada, m_norm_ffn1, m_w_ffn1_in, m_w_ffn1_out, m_norm_mix, m_w_in, m_pool_grp, m_pool_scale, m_w_pool_proj, m_q_a_norm, m_w_q_up, m_kv_a_norm, m_w_kv_up, m_q_norm_nope, m_q_norm_rope, m_k_norm_nope, m_k_norm_rope, m_w_mla_proj, m_w_out, m_norm_ffn2, m_w_ffn2_in, m_w_ffn2_out, v_w_ada, v_b_ada, v_norm_ffn1, v_w_ffn1_in, v_w_ffn1_out, v_norm_mix, v_w_in, v_pool_grp, v_pool_scale, v_w_pool_proj, v_q_a_norm, v_w_q_up, v_kv_a_norm, v_w_kv_up, v_q_norm_nope, v_q_norm_rope, v_k_norm_nope, v_k_norm_rope, v_w_mla_proj, v_w_out, v_norm_ffn2, v_w_ffn2_in, v_w_ffn2_out):
    args = dict(locals())
    wts = {n: args[n][0] for n in WEIGHTS}
    mom = {n: args["m_" + n][0] for n in WEIGHTS}
    var = {n: args["v_" + n][0] for n in WEIGHTS}
    nb, seq, dm = x.shape
    tokens = nb * seq
    xi, yi, ci = _coords()
    chip = 2 * xi + yi

    cidx = ci.astype(jnp.int32).reshape(1)
    plan = _ExchangePlan({n: _pad_rows(wts[n].T.astype(MX)) if n in TRANSPOSED else wts[n].astype(MX) for n in BIG}, cidx)
    plan.W["pool_grp"] = wts["pool_grp"].astype(MX)

    ncol = w_ada.shape[2]
    b_cols = lax.dynamic_slice_in_dim(wts["b_ada"].reshape(1, -1), chip * ncol, ncol, axis=1)
    c_slots, mod_slots = _run(plan, _ada_prologue, c, wts["w_ada"], b_cols, name="ada_prologue")
    c_all = c_slots[:, :nb].reshape(N_DEV * nb, dm)
    mod3 = jnp.transpose(mod_slots[:, :nb], (1, 0, 2)).reshape(nb, 9, dm)
    P = {"norm_ffn1": wts["norm_ffn1"].reshape(1, dm), "norm_mix": wts["norm_mix"].reshape(1, dm),
         "norm_ffn2": wts["norm_ffn2"].reshape(1, dm), "pool_scale": wts["pool_scale"].reshape(1, POOL_W),
         "q_a_norm": wts["q_a_norm"].reshape(1, QL), "kv_a_norm": wts["kv_a_norm"].reshape(1, KVL),
         "q_gain": _gain_slab(wts["q_norm_nope"].reshape(1, NOPE), wts["q_norm_rope"].reshape(1, ROPE)),
         "k_gain": _gain_slab(wts["k_norm_nope"].reshape(1, NOPE), wts["k_norm_rope"].reshape(1, ROPE))}
    cos, sin = _rope_tables(positions.reshape(tokens))

    loss8, grad_x, stats, d_pool_grp = _layer_fwd_bwd(x.reshape(tokens, dm), loss_target.reshape(tokens, dm), mod3, cos, sin, plan, P)

    slab = _pack_stats(*stats, loss8, name="pack_stats")
    red, dmod_rows, pool_total = _run(plan, _small_allreduce, slab, d_pool_grp.reshape(POOL_G * LANE, LANE), name="small_allreduce")
    grads_t = {n: plan.joined[n][:W_IN_SHARD] for n in TRANSPOSED}
    grads = {n: grads_t[n].T if n in TRANSPOSED else plan.joined[n] for n in BIG if n != "w_ffn1_in"}
    grads["w_ffn1_in"] = jnp.concatenate([plan.joined["w_ffn1_in@0"], plan.joined["w_ffn1_in@1"]], axis=0)
    dm_rows = dmod_rows.shape[0] // N_DEV
    dmod_all = dmod_rows.reshape(N_DEV, dm_rows, SLAB_W)[:, :9 * nb].reshape(N_DEV * nb, 9 * dm)
    dmod_cols = lax.dynamic_slice_in_dim(dmod_all, chip * ncol, ncol, axis=1)
    grads["w_ada"] = _ada_bwd(c_all, dmod_cols, name="ada_bwd")

    delta, new_m, new_v = {}, {}, {}
    as2d = lambda a: a.reshape(POOL_G * LANE, LANE) if a.ndim == 4 else a.reshape(1, -1)
    upd, loss_row = _small_update(red, dmod_rows, pool_total, nb,
                                  {n: tuple(as2d(args[p + n]) for p in ("", "m_", "v_")) for n in SMALL}, name="small_update")
    loss = loss_row[0, 0]
    for n in SMALL:
        grads[n], delta[n], new_m[n], new_v[n] = upd[n]
    for n in ("w_ada",) + BIG:
        if n in NARROW:
            res = _adamw(wts[n].T, grads_t[n] if n in TRANSPOSED else grads[n].T, mom[n].T, var[n].T, name="adamw_" + n)
            delta[n], new_m[n], new_v[n] = (r.T for r in res)
        else:
            delta[n], new_m[n], new_v[n] = _adamw(wts[n], grads[n], mom[n], var[n], name="adamw_" + n)

    def lead(a, n):
        return a.reshape((1,) + wts[n].shape)

    return (loss, grad_x.reshape(nb, seq, dm), *[lead(grads[n], n) for n in WEIGHTS], *[lead(delta[n], n) for n in WEIGHTS],
            *[lead(new_m[n], n) for n in WEIGHTS], *[lead(new_v[n], n) for n in WEIGHTS])
```

```python
import functools
import math

import jax
import jax.numpy as jnp
from jax import lax
from jax.experimental import pallas as pl
from jax.experimental.pallas import tpu as pltpu

F32 = jnp.float32
MX = jnp.bfloat16

D = 1024
DFF = 2816
NH = 8
POOL_W = 512
POOL_G = 4
QL = 384
KVL = 256
ROPE = 32
NOPE = 64
LANE = 128
SUBLANE = 8
EPS = 1e-6
ATTN_SCALE = 1.0 / math.sqrt(96.0)
NEG = -1e30

Z_UP, Z_QL, Z_KV, Z_KR, Z_GP, Z_GM, Z_W = 0, 512, 896, 1152, 1280, 2304, 3328
KR_LANE = 64
LAT_W = 1280

ADAM_LR, ADAM_B1, ADAM_B2, ADAM_EPS, ADAM_WD, ADAM_STEP = 0.001, 0.9, 0.999, 1e-08, 0.01, 10

VMEM_LIMIT = 48 * 1024 * 1024
MESH = pl.DeviceIdType.MESH
N_DEV = 8
N_CHIP = 4


class _Comm:
    def __init__(self, ins, out_shapes, sems, start, finish, aliases=None):
        self.ins, self.out_shapes, self.sems = list(ins), list(out_shapes), list(sems)
        self.start, self.finish = start, finish
        self.aliases = aliases or {}


def _pcall(body, *, name, out_shape, grid=(), in_specs=None, out_specs=None, scratch=(), grid_spec=None, aliases=None,
           comm=None):
    params = pltpu.CompilerParams(vmem_limit_bytes=VMEM_LIMIT)
    kw = dict(name=name, compiler_params=params)
    if comm is None:
        if aliases:
            kw["input_output_aliases"] = aliases
        if grid_spec is not None:
            return pl.pallas_call(body, grid_spec=grid_spec, out_shape=out_shape, **kw)
        return pl.pallas_call(body, grid=grid, in_specs=in_specs, out_specs=out_specs, scratch_shapes=scratch,
                              out_shape=out_shape, **kw)
    single = not isinstance(out_shape, (list, tuple))
    outs = [out_shape] if single else list(out_shape)
    ospecs = [out_specs] if single else list(out_specs)
    n_in, n_out, n_ci, n_co, n_scr = len(in_specs), len(outs), len(comm.ins), len(comm.out_shapes), len(scratch)
    io = dict(aliases or {})
    io.update({n_in + i: n_out + o for i, o in comm.aliases.items()})

    def riding(*refs):
        ins, cins = refs[:n_in], refs[n_in:n_in + n_ci]
        at = n_in + n_ci
        os_, couts = refs[at:at + n_out], refs[at + n_out:at + n_out + n_co]
        at += n_out + n_co
        scr, csems = refs[at:at + n_scr], refs[at + n_scr:]
        if grid:
            first = functools.reduce(jnp.logical_and, [pl.program_id(d) == 0 for d in range(len(grid))])
            last = functools.reduce(jnp.logical_and, [pl.program_id(d) == grid[d] - 1 for d in range(len(grid))])
            pl.when(first)(lambda: comm.start(cins, couts, csems))
            body(*ins, *os_, *scr)
            pl.when(last)(lambda: comm.finish(cins, couts, csems))
        else:
            comm.start(cins, couts, csems)
            body(*ins, *os_, *scr)
            comm.finish(cins, couts, csems)

    call = pl.pallas_call(riding, grid=grid, in_specs=list(in_specs) + [HBM_SPEC] * n_ci, out_specs=ospecs + [HBM_SPEC] * n_co,
                          out_shape=outs + comm.out_shapes, scratch_shapes=list(scratch) + comm.sems,
                          input_output_aliases=io, **kw)

    def run(*args):
        res = call(*args, *comm.ins)
        main = list(res[:n_out])
        return (main[0] if single else main), list(res[n_out:])

    return run


def _pick(dim, target):
    best = None
    for t in range(LANE, min(dim, target) + 1, LANE):
        if dim % t == 0:
            best = t
    return dim if best is None else best


def _sds(shape, dtype):
    return jax.ShapeDtypeStruct(shape, dtype)


def _dw(a, g, *, name, tm=512, tn=1024, out_t=False, comm=None):
    return _mm(a, g, mode="tn", out_dtype=F32, name=name, tm=tm, tn=tn, tk=a.shape[0], n_outer=True, out_t=out_t, comm=comm)


def _mm(a, b, *, mode, out_dtype, name, tm=1024, tn=1024, tk=4096, n_outer=False, out_t=False, comm=None):
    if mode == "nn":
        (M, K), (K2, N) = a.shape, b.shape
    elif mode == "nt":
        (M, K), (N, K2) = a.shape, b.shape
    else:
        (K, M), (K2, N) = a.shape, b.shape
    assert K == K2, (name, a.shape, b.shape)
    tm, tn, tk = _pick(M, tm), _pick(N, tn), _pick(K, tk)
    nk = K // tk
    if n_outer:
        ij = lambda g0, g1: (g1, g0)
        grid = (N // tn, M // tm, nk)
    else:
        ij = lambda g0, g1: (g0, g1)
        grid = (M // tm, N // tn, nk)
    if mode == "tn":
        a_spec = pl.BlockSpec((tk, tm), lambda g0, g1, k: (k, ij(g0, g1)[0]))
    else:
        a_spec = pl.BlockSpec((tm, tk), lambda g0, g1, k: (ij(g0, g1)[0], k))
    if mode == "nt":
        b_spec = pl.BlockSpec((tn, tk), lambda g0, g1, k: (ij(g0, g1)[1], k))
    else:
        b_spec = pl.BlockSpec((tk, tn), lambda g0, g1, k: (k, ij(g0, g1)[1]))
    if out_t:
        assert nk == 1, name
        o_spec = pl.BlockSpec((tn, tm), lambda g0, g1, k: ij(g0, g1)[::-1])
    else:
        o_spec = pl.BlockSpec((tm, tn), lambda g0, g1, k: ij(g0, g1))
    dn = {"nn": (((1,), (0,)), ((), ())), "nt": (((1,), (1,)), ((), ())), "tn": (((0,), (0,)), ((), ()))}[mode]

    def dot(a_ref, b_ref):
        return lax.dot_general(a_ref[...].astype(MX), b_ref[...].astype(MX), dn, preferred_element_type=F32)

    def body_one(a_ref, b_ref, o_ref):
        prod = dot(a_ref, b_ref)
        o_ref[...] = (prod.T if out_t else prod).astype(o_ref.dtype)

    def body_acc(a_ref, b_ref, o_ref, acc_ref):
        k = pl.program_id(2)
        part = dot(a_ref, b_ref)

        @pl.when(k == 0)
        def _():
            acc_ref[...] = part

        @pl.when(k > 0)
        def _():
            acc_ref[...] += part

        @pl.when(k == nk - 1)
        def _():
            o_ref[...] = acc_ref[...].astype(o_ref.dtype)

    return _pcall(body_one if nk == 1 else body_acc, name=name, out_shape=_sds((N, M) if out_t else (M, N), out_dtype), grid=grid,
                  in_specs=[a_spec, b_spec], out_specs=o_spec, scratch=[] if nk == 1 else [pltpu.VMEM((tm, tn), F32)],
                  comm=comm)(a, b)


def _gu_shard(q):
    return (q % 2) * 2 + q // 2


def _ffn_up(h, w_st, *, name, tm=512, comm=None):
    T, dm = h.shape
    hw = w_st.shape[2]
    tm = _pick(T, tm)

    def body(h_ref, wg_ref, wu_ref, gu_ref, a_ref):
        hv = h_ref[...]
        g = jnp.dot(hv, wg_ref[0], preferred_element_type=F32)
        u = jnp.dot(hv, wu_ref[0], preferred_element_type=F32)
        gu_ref[:, :hw] = g.astype(gu_ref.dtype)
        gu_ref[:, hw:] = u.astype(gu_ref.dtype)
        a_ref[...] = (g * _sigmoid(g) * u).astype(a_ref.dtype)

    return _pcall(body, name=name, grid=(T // tm, 2),
                  in_specs=[pl.BlockSpec((tm, dm), lambda i, j: (i, 0)), pl.BlockSpec((1, dm, hw), lambda i, j: (j, 0, 0)),
                            pl.BlockSpec((1, dm, hw), lambda i, j: (2 + j, 0, 0))],
                  out_specs=[pl.BlockSpec((tm, 2 * hw), lambda i, j: (i, j)), pl.BlockSpec((tm, hw), lambda i, j: (i, j))],
                  out_shape=[_sds((T, 4 * hw), MX), _sds((T, 2 * hw), MX)], comm=comm)(h, w_st, w_st)


def _ffn_up_first(x, mod3, gain, w_st, *, sub, name, tm=512, comm=None):
    T, dm = x.shape
    hw = w_st.shape[2]
    tm = _pick(T, tm)
    tps = (T // mod3.shape[0]) // tm

    def body(x_ref, mod_ref, n_ref, wg_ref, wu_ref, h_ref, gu_ref, a_ref):
        xv = x_ref[...]
        xn = xv * _rsq(xv) * n_ref[...]
        hv = (xn * (1.0 + mod_ref[0, 3 * sub + 1:3 * sub + 2, :]) + mod_ref[0, 3 * sub:3 * sub + 1, :]).astype(h_ref.dtype)
        h_ref[...] = hv
        g = jnp.dot(hv, wg_ref[0], preferred_element_type=F32)
        u = jnp.dot(hv, wu_ref[0], preferred_element_type=F32)
        gu_ref[:, :hw] = g.astype(gu_ref.dtype)
        gu_ref[:, hw:] = u.astype(gu_ref.dtype)
        a_ref[...] = (g * _sigmoid(g) * u).astype(a_ref.dtype)

    return _pcall(body, name=name, grid=(T // tm,),
                  in_specs=[_row_spec(tm, dm), pl.BlockSpec((1, 9, dm), lambda i: (i // tps, 0, 0)),
                            pl.BlockSpec((1, dm), lambda i: (0, 0)), pl.BlockSpec((1, dm, hw), lambda i: (0, 0, 0)),
                            pl.BlockSpec((1, dm, hw), lambda i: (2, 0, 0))],
                  out_specs=[_row_spec(tm, dm), pl.BlockSpec((tm, 2 * hw), lambda i: (i, 0)), pl.BlockSpec((tm, hw), lambda i: (i, 0))],
                  out_shape=[_sds((T, dm), MX), _sds((T, 4 * hw), MX), _sds((T, 2 * hw), MX)],
                  comm=comm)(x, mod3, gain, w_st, w_st)


def _ffn_up_second(h, w_st, gu, a, *, name, tm=512, comm=None):
    T, dm = h.shape
    hw = w_st.shape[2]
    tm = _pick(T, tm)

    def body(h_ref, wg_ref, wu_ref, gu_in, a_in, gu_ref, a_ref):
        hv = h_ref[...]
        g = jnp.dot(hv, wg_ref[0], preferred_element_type=F32)
        u = jnp.dot(hv, wu_ref[0], preferred_element_type=F32)
        gu_ref[:, :hw] = g.astype(gu_ref.dtype)
        gu_ref[:, hw:] = u.astype(gu_ref.dtype)
        a_ref[...] = (g * _sigmoid(g) * u).astype(a_ref.dtype)

    return _pcall(body, name=name, grid=(T // tm,),
                  in_specs=[_row_spec(tm, dm), pl.BlockSpec((1, dm, hw), lambda i: (1, 0, 0)),
                            pl.BlockSpec((1, dm, hw), lambda i: (3, 0, 0)), HBM_SPEC, HBM_SPEC],
                  out_specs=[pl.BlockSpec((tm, 2 * hw), lambda i: (i, 1)), pl.BlockSpec((tm, hw), lambda i: (i, 1))],
                  out_shape=[_sds(gu.shape, gu.dtype), _sds(a.shape, a.dtype)], aliases={3: 0, 4: 1},
                  comm=comm)(h, w_st, w_st, gu, a)


def _ffn_dact(df, gu, w_out, *, name, tm=512, comm=None):
    T, dm = df.shape
    hw = gu.shape[1] // 4
    tm = _pick(T, tm)

    def body(df_ref, gu_ref, wo_ref, dgu_ref):
        da = lax.dot_general(df_ref[...], wo_ref[...], (((1,), (1,)), ((), ())), preferred_element_type=F32)
        g = gu_ref[:, :hw].astype(F32)
        u = gu_ref[:, hw:].astype(F32)
        s = _sigmoid(g)
        dgu_ref[:, :hw] = (da * u * (s * (1.0 + g * (1.0 - s)))).astype(dgu_ref.dtype)
        dgu_ref[:, hw:] = (da * (g * s)).astype(dgu_ref.dtype)

    return _pcall(body, name=name, grid=(T // tm, 2),
                  in_specs=[pl.BlockSpec((tm, dm), lambda i, j: (i, 0)), pl.BlockSpec((tm, 2 * hw), lambda i, j: (i, j)),
                            pl.BlockSpec((hw, dm), lambda i, j: (j, 0))],
                  out_specs=pl.BlockSpec((tm, 2 * hw), lambda i, j: (i, j)), out_shape=_sds(gu.shape, MX),
                  comm=comm)(df, gu, w_out)


def _ffn_dh(dgu, w_st, *, name, tm=1024, comm=None):
    T = dgu.shape[0]
    _, dm, hw = w_st.shape
    tm = _pick(T, tm)

    def body(d_ref, w_ref, o_ref, acc_ref):
        q = pl.program_id(1)
        part = lax.dot_general(d_ref[...], w_ref[0], (((1,), (1,)), ((), ())), preferred_element_type=F32)

        @pl.when(q == 0)
        def _():
            acc_ref[...] = part

        @pl.when(jnp.logical_and(q > 0, q < 3))
        def _():
            acc_ref[...] += part

        @pl.when(q == 3)
        def _():
            o_ref[...] = acc_ref[...] + part

    return _pcall(body, name=name, grid=(T // tm, 4),
                  in_specs=[pl.BlockSpec((tm, hw), lambda i, q: (i, q)), pl.BlockSpec((1, dm, hw), lambda i, q: (_gu_shard(q), 0, 0))],
                  out_specs=pl.BlockSpec((tm, dm), lambda i, q: (i, 0)), out_shape=_sds((T, dm), F32),
                  scratch=[pltpu.VMEM((tm, dm), F32)], comm=comm)(dgu, w_st)


def _ffn_dw_in(h, dgu, *, name, rows=None, tm=512, comm=None):
    T, dm = h.shape
    hw = dgu.shape[1] // 4
    first, count = rows if rows is not None else (0, dm)
    tm = _pick(count, tm)
    skip = first // tm

    def body(h_ref, d_ref, o_ref, ht_ref):
        @pl.when(pl.program_id(1) == 0)
        def _():
            ht_ref[...] = h_ref[...].T

        o_ref[0] = jnp.dot(ht_ref[...], d_ref[...], preferred_element_type=F32)

    return _pcall(body, name=name, grid=(count // tm, 4),
                  in_specs=[pl.BlockSpec((T, tm), lambda i, q: (0, skip + i)), pl.BlockSpec((T, hw), lambda i, q: (0, q))],
                  out_specs=pl.BlockSpec((1, tm, hw), lambda i, q: (_gu_shard(q), i, 0)),
                  out_shape=_sds((4, count, hw), F32), scratch=[pltpu.VMEM((tm, T), h.dtype)], comm=comm)(h, dgu)


def _rsq(x):
    return lax.rsqrt(jnp.mean(x * x, axis=-1, keepdims=True) + EPS)


def _sigmoid(x):
    return 1.0 / (1.0 + jnp.exp(-x))


def _row_spec(tm, w):
    return pl.BlockSpec((tm, w), lambda i: (i, 0))


def _mm_resid_norm(a, w, x_prev, mod3, gain, *, sub, coef, name, tm=512, comm=None):
    T, k = a.shape
    dm = w.shape[1]
    tm = _pick(T, tm)
    tps = (T // mod3.shape[0]) // tm

    def body(a_ref, w_ref, x_ref, mod_ref, n_ref, f_ref, xo_ref, h_ref):
        f = jnp.dot(a_ref[...], w_ref[...], preferred_element_type=F32)
        f_ref[...] = f
        x = x_ref[...] + coef * mod_ref[0, 3 * sub - 1:3 * sub, :] * f
        xo_ref[...] = x
        xn = x * _rsq(x) * n_ref[...]
        h_ref[...] = (xn * (1.0 + mod_ref[0, 3 * sub + 1:3 * sub + 2, :]) + mod_ref[0, 3 * sub:3 * sub + 1, :]).astype(h_ref.dtype)

    row = _row_spec(tm, dm)
    return _pcall(body, name=name, grid=(T // tm,),
                  in_specs=[_row_spec(tm, k), pl.BlockSpec((k, dm), lambda i: (0, 0)), row,
                            pl.BlockSpec((1, 9, dm), lambda i: (i // tps, 0, 0)), pl.BlockSpec((1, dm), lambda i: (0, 0))],
                  out_specs=[row, row, row], out_shape=[_sds((T, dm), F32), _sds((T, dm), F32), _sds((T, dm), MX)],
                  comm=comm)(a, w, x_prev, mod3, gain)


def _mm_loss(a, w, x2, tgt, mod3, *, name, tm=512):
    T, k = a.shape
    dm = w.shape[1]
    tm = _pick(T, tm)
    tps = (T // mod3.shape[0]) // tm
    mod_spec = pl.BlockSpec((1, 9, dm), lambda i: (i // tps, 0, 0))
    row = _row_spec(tm, dm)
    stat_spec = pl.BlockSpec((1, SUBLANE, dm), lambda i: (i // tps, 0, 0))
    loss_spec = pl.BlockSpec((SUBLANE, LANE), lambda i: (0, 0))

    def body(a_ref, w_ref, x_ref, t_ref, mod_ref, dy_ref, df_ref, st_ref, loss_ref):
        i = pl.program_id(0)
        g = mod_ref[0, 8:9, :]
        f = jnp.dot(a_ref[...], w_ref[...], preferred_element_type=F32)
        err = x_ref[...] + 0.5 * g * f - t_ref[...]
        dy = err * (1.0 / dm)
        dy_ref[...] = dy
        df_ref[...] = (0.5 * g * dy).astype(df_ref.dtype)
        dgate = jnp.sum(0.5 * dy * f, axis=0, keepdims=True)
        part = 0.5 * jnp.sum(jnp.sum(err * err, axis=0, keepdims=True), axis=1, keepdims=True) * (1.0 / dm)

        @pl.when(i % tps == 0)
        def _():
            st_ref[...] = jnp.zeros_like(st_ref)

        @pl.when(i == 0)
        def _():
            loss_ref[...] = jnp.zeros_like(loss_ref)

        st_ref[0, 0:1, :] += dgate
        loss_ref[...] += jnp.broadcast_to(part, loss_ref.shape)

    return _pcall(body, name=name, grid=(T // tm,),
                  in_specs=[_row_spec(tm, k), pl.BlockSpec((k, dm), lambda i: (0, 0)), row, row, mod_spec],
                  out_specs=[row, row, stat_spec, loss_spec],
                  out_shape=[_sds((T, dm), F32), _sds((T, dm), MX), _sds((mod3.shape[0], SUBLANE, dm), F32),
                             _sds((SUBLANE, LANE), F32)])(a, w, x2, tgt, mod3)


def _norm_bwd_tail(dhv, x_ref, dxi_ref, f_ref, mod_ref, n_ref, dx_ref, df_ref, st_ref, *, first, sub, coef):
    x = x_ref[...]
    r = _rsq(x)
    xhat = x * r
    n = n_ref[...]
    d_shift = jnp.sum(dhv, axis=0, keepdims=True)
    d_scale = jnp.sum(dhv * (xhat * n), axis=0, keepdims=True)
    dxn = dhv * (1.0 + mod_ref[0, 3 * sub + 1:3 * sub + 2, :])
    d_gain = jnp.sum(dxn * xhat, axis=0, keepdims=True)
    dxhat = dxn * n
    dx = dxi_ref[...] + r * (dxhat - xhat * jnp.mean(dxhat * xhat, axis=-1, keepdims=True))
    dx_ref[...] = dx

    @pl.when(first)
    def _():
        st_ref[...] = jnp.zeros_like(st_ref)

    st_ref[0, 0:1, :] += d_shift
    st_ref[0, 1:2, :] += d_scale
    st_ref[0, 2:3, :] += d_gain
    if f_ref is not None:
        st_ref[0, 3:4, :] += jnp.sum(coef * dx * f_ref[...], axis=0, keepdims=True)
        df_ref[...] = (coef * mod_ref[0, 3 * sub - 1:3 * sub, :] * dx).astype(df_ref.dtype)


def _ffn_dh_norm(dgu, w_st, x_cur, dx_in, f_prev, mod3, gain, *, sub, coef, name, tm=512, comm=None):
    T = dgu.shape[0]
    _, dm, hw = w_st.shape
    tm = _pick(T, tm)
    nb = mod3.shape[0]
    tps = (T // nb) // tm

    def body(d_ref, w_ref, x_ref, dxi_ref, f_ref, mod_ref, n_ref, dx_ref, df_ref, st_ref, acc_ref):
        i = pl.program_id(0)
        q = pl.program_id(1)
        part = lax.dot_general(d_ref[...], w_ref[0], (((1,), (1,)), ((), ())), preferred_element_type=F32)

        @pl.when(q == 0)
        def _():
            acc_ref[...] = part

        @pl.when(q > 0)
        def _():
            acc_ref[...] += part

        @pl.when(q == 3)
        def _():
            _norm_bwd_tail(acc_ref[...], x_ref, dxi_ref, f_ref, mod_ref, n_ref, dx_ref, df_ref, st_ref,
                           first=i % tps == 0, sub=sub, coef=coef)

    row = pl.BlockSpec((tm, dm), lambda i, q: (i, 0))
    return _pcall(body, name=name, grid=(T // tm, 4),
                  in_specs=[pl.BlockSpec((tm, hw), lambda i, q: (i, q)), pl.BlockSpec((1, dm, hw), lambda i, q: (_gu_shard(q), 0, 0)),
                            row, row, row, pl.BlockSpec((1, 9, dm), lambda i, q: (i // tps, 0, 0)),
                            pl.BlockSpec((1, dm), lambda i, q: (0, 0))],
                  out_specs=[row, row, pl.BlockSpec((1, SUBLANE, dm), lambda i, q: (i // tps, 0, 0))],
                  out_shape=[_sds((T, dm), F32), _sds((T, dm), MX), _sds((nb, SUBLANE, dm), F32)],
                  scratch=[pltpu.VMEM((tm, dm), F32)], comm=comm)(dgu, w_st, x_cur, dx_in, f_prev, mod3, gain)


def _mm_norm_bwd(a, b, x_cur, dx_in, f_prev, mod3, gain, *, sub, coef, name, tm=512, comm=None):
    T, k = a.shape
    dm = b.shape[1]
    tm = _pick(T, tm)
    nb = mod3.shape[0]
    tps = (T // nb) // tm

    def body(a_ref, b_ref, x_ref, dxi_ref, f_ref, mod_ref, n_ref, dx_ref, df_ref, st_ref):
        dh = jnp.dot(a_ref[...], b_ref[...], preferred_element_type=F32)
        _norm_bwd_tail(dh, x_ref, dxi_ref, f_ref, mod_ref, n_ref, dx_ref, df_ref, st_ref,
                       first=pl.program_id(0) % tps == 0, sub=sub, coef=coef)

    row = _row_spec(tm, dm)
    return _pcall(body, name=name, grid=(T // tm,),
                  in_specs=[_row_spec(tm, k), pl.BlockSpec((k, dm), lambda i: (0, 0)), row, row, row,
                            pl.BlockSpec((1, 9, dm), lambda i: (i // tps, 0, 0)), pl.BlockSpec((1, dm), lambda i: (0, 0))],
                  out_specs=[row, row, pl.BlockSpec((1, SUBLANE, dm), lambda i: (i // tps, 0, 0))],
                  out_shape=[_sds((T, dm), F32), _sds((T, dm), MX), _sds((nb, SUBLANE, dm), F32)],
                  comm=comm)(a, b, x_cur, dx_in, f_prev, mod3, gain)


def _bwd_block(x_cur, dh, dx_in, f_prev, mod3, gain, *, sub, coef, name, tm=256, comm=None):
    T, dm = x_cur.shape
    nb = mod3.shape[0]
    tps = (T // nb) // tm
    has_f = f_prev is not None
    mod_spec = pl.BlockSpec((1, 9, dm), lambda i: (i // tps, 0, 0))
    vec_spec = pl.BlockSpec((1, dm), lambda i: (0, 0))
    stat_spec = pl.BlockSpec((1, SUBLANE, dm), lambda i: (i // tps, 0, 0))
    row = _row_spec(tm, dm)

    def body(*refs):
        if has_f:
            x_ref, dh_ref, dxi_ref, f_ref, mod_ref, n_ref, dx_ref, df_ref, st_ref = refs
        else:
            x_ref, dh_ref, dxi_ref, mod_ref, n_ref, dx_ref, st_ref = refs
            f_ref = df_ref = None
        _norm_bwd_tail(dh_ref[...], x_ref, dxi_ref, f_ref, mod_ref, n_ref, dx_ref, df_ref, st_ref,
                       first=pl.program_id(0) % tps == 0, sub=sub, coef=coef)

    st_shape = _sds((nb, SUBLANE, dm), F32)
    if has_f:
        return _pcall(body, name=name, grid=(T // tm,), in_specs=[row, row, row, row, mod_spec, vec_spec],
                      out_specs=[row, row, stat_spec],
                      out_shape=[_sds((T, dm), F32), _sds((T, dm), MX), st_shape], comm=comm)(x_cur, dh, dx_in, f_prev, mod3, gain)
    return _pcall(body, name=name, grid=(T // tm,), in_specs=[row, row, row, mod_spec, vec_spec],
                  out_specs=[row, stat_spec], out_shape=[_sds((T, dm), F32), st_shape], comm=comm)(x_cur, dh, dx_in, mod3, gain)


def _mix_out_fwd(z, br_pool, attn, w_mla, w_out, x_prev, mod3, gain, *, sub, name, tm=512):
    T = z.shape[0]
    dm = w_out.shape[1]
    tm = _pick(T, tm)
    tps = (T // mod3.shape[0]) // tm
    whole = lambda a: pl.BlockSpec(a.shape, lambda i: (0, 0))

    def body(z_ref, bp_ref, at_ref, wm_ref, wo_ref, x_ref, mod_ref, n_ref, bm_ref, mg_ref, mo_ref, xo_ref, h_ref):
        bm = jnp.dot(at_ref[...], wm_ref[...], preferred_element_type=F32).astype(MX)
        bm_ref[...] = bm
        gp = z_ref[:, Z_GP:Z_GP + D].astype(F32)
        gm = z_ref[:, Z_GM:Z_GM + D].astype(F32)
        merged = (_sigmoid(gp) * bp_ref[...] + _sigmoid(gm) * bm).astype(MX)
        mg_ref[...] = merged
        mo = jnp.dot(merged, wo_ref[...], preferred_element_type=F32)
        mo_ref[...] = mo
        x = x_ref[...] + mod_ref[0, 3 * sub - 1:3 * sub, :] * mo
        xo_ref[...] = x
        xn = x * _rsq(x) * n_ref[...]
        h_ref[...] = (xn * (1.0 + mod_ref[0, 3 * sub + 1:3 * sub + 2, :]) + mod_ref[0, 3 * sub:3 * sub + 1, :]).astype(h_ref.dtype)

    row = _row_spec(tm, dm)
    return _pcall(body, name=name, grid=(T // tm,),
                  in_specs=[_row_spec(tm, Z_W), row, _row_spec(tm, attn.shape[1]), whole(w_mla), whole(w_out), row,
                            pl.BlockSpec((1, 9, dm), lambda i: (i // tps, 0, 0)), pl.BlockSpec((1, dm), lambda i: (0, 0))],
                  out_specs=[row, row, row, row, row],
                  out_shape=[_sds((T, dm), MX), _sds((T, dm), MX), _sds((T, dm), F32), _sds((T, dm), F32), _sds((T, dm), MX)],
                  )(z, br_pool, attn, w_mla, w_out, x_prev, mod3, gain)


def _mix_out_bwd(dmo, w_out, z, br_pool, br_mla, w_mla, *, name, tm=512):
    T = z.shape[0]
    tm = _pick(T, tm)
    whole = lambda a: pl.BlockSpec(a.shape, lambda i: (0, 0))
    nt = (((1,), (1,)), ((), ()))

    def body(dmo_ref, wo_ref, z_ref, bp_ref, bm_ref, wm_ref, dbp_ref, dbm_ref, dg_ref, dat_ref):
        dm = lax.dot_general(dmo_ref[...], wo_ref[...], nt, preferred_element_type=F32)
        sp = _sigmoid(z_ref[:, Z_GP:Z_GP + D].astype(F32))
        sm = _sigmoid(z_ref[:, Z_GM:Z_GM + D].astype(F32))
        dbp_ref[...] = (dm * sp).astype(dbp_ref.dtype)
        dbm = (dm * sm).astype(MX)
        dbm_ref[...] = dbm
        dg_ref[:, :D] = (dm * bp_ref[...].astype(F32) * sp * (1.0 - sp)).astype(dg_ref.dtype)
        dg_ref[:, D:] = (dm * bm_ref[...].astype(F32) * sm * (1.0 - sm)).astype(dg_ref.dtype)
        dat_ref[...] = lax.dot_general(dbm, wm_ref[...], nt, preferred_element_type=F32)

    row = _row_spec(tm, D)
    return _pcall(body, name=name, grid=(T // tm,),
                  in_specs=[row, whole(w_out), _row_spec(tm, Z_W), row, row, whole(w_mla)],
                  out_specs=[row, row, _row_spec(tm, 2 * D), _row_spec(tm, w_mla.shape[0])],
                  out_shape=[_sds((T, D), MX), _sds((T, D), MX), _sds((T, 2 * D), MX), _sds((T, w_mla.shape[0]), F32)],
                  )(dmo, w_out, z, br_pool, br_mla, w_mla)


def _shift_down(x, k, row):
    return jnp.where(row >= k, pltpu.roll(x, k, 0), 0.0)


def _shift_up(x, k, row, n):
    return jnp.where(row < n - k, pltpu.roll(x, n - k, 0), 0.0)


def _pool_fwd(z, pool_grp, pool_scale, *, nb, name):
    T = z.shape[0]
    S = T // nb
    blk = pl.BlockSpec((S, LANE), lambda b, g: (b, g))

    def body(u_ref, w_ref, s_ref, pooled_ref, mixed_ref, scaled_ref):
        g = pl.program_id(1)
        u = u_ref[...].astype(F32)
        row = lax.broadcasted_iota(jnp.int32, u.shape, 0)
        s2 = u + _shift_down(u, 1, row)
        s4 = s2 + _shift_down(s2, 2, row)
        s8 = s4 + _shift_down(s4, 4, row)
        s16 = s8 + _shift_down(s8, 8, row)
        win = jnp.where(g == 0, s2, jnp.where(g == 1, s4, jnp.where(g == 2, s8, s16)))
        width = lax.shift_left(jnp.int32(2), g)
        cnt = jnp.minimum(row + 1, width).astype(F32)
        pooled = (win / cnt - u).astype(MX)
        pooled_ref[...] = pooled
        mixed = jnp.dot(pooled, w_ref[0], preferred_element_type=F32)
        mixed_ref[...] = mixed
        scaled_ref[...] = (mixed * s_ref[...]).astype(scaled_ref.dtype)

    return _pcall(body, name=name, grid=(nb, POOL_G),
                  in_specs=[blk, pl.BlockSpec((1, LANE, LANE), lambda b, g: (g, 0, 0)),
                            pl.BlockSpec((1, LANE), lambda b, g: (0, g))],
                  out_specs=[blk, blk, blk],
                  out_shape=[_sds((T, POOL_W), MX), _sds((T, POOL_W), F32), _sds((T, POOL_W), MX)])(z, pool_grp, pool_scale)


def _pool_bwd(dscaled, mixed, pooled, pool_grp, pool_scale, *, nb, name):
    T = dscaled.shape[0]
    S = T // nb
    blk = pl.BlockSpec((S, LANE), lambda g, b: (b, g))

    def body(ds_ref, mixed_ref, pooled_ref, w_ref, s_ref, du_ref, dw_ref, dsc_ref):
        g = pl.program_id(0)
        b = pl.program_id(1)
        ds = ds_ref[...]
        dsc_ref[0] = jnp.sum(ds * mixed_ref[...], axis=0, keepdims=True)
        dmixed = (ds * s_ref[...]).astype(MX)
        dw = lax.dot_general(pooled_ref[...], dmixed, (((0,), (0,)), ((), ())), preferred_element_type=F32)

        @pl.when(b == 0)
        def _():
            dw_ref[0] = dw

        @pl.when(b > 0)
        def _():
            dw_ref[0] += dw
        dpooled = lax.dot_general(dmixed, w_ref[0], (((1,), (1,)), ((), ())), preferred_element_type=F32)
        row = lax.broadcasted_iota(jnp.int32, dpooled.shape, 0)
        width = lax.shift_left(jnp.int32(2), g)
        q = dpooled / jnp.minimum(row + 1, width).astype(F32)
        r2 = q + _shift_up(q, 1, row, S)
        r4 = r2 + _shift_up(r2, 2, row, S)
        r8 = r4 + _shift_up(r4, 4, row, S)
        r16 = r8 + _shift_up(r8, 8, row, S)
        win = jnp.where(g == 0, r2, jnp.where(g == 1, r4, jnp.where(g == 2, r8, r16)))
        du_ref[...] = (win - dpooled).astype(du_ref.dtype)

    return _pcall(body, name=name, grid=(POOL_G, nb),
                  in_specs=[blk, blk, blk, pl.BlockSpec((1, LANE, LANE), lambda g, b: (g, 0, 0)),
                            pl.BlockSpec((1, LANE), lambda g, b: (0, g))],
                  out_specs=[blk, pl.BlockSpec((1, LANE, LANE), lambda g, b: (g, 0, 0)),
                             pl.BlockSpec((1, 1, LANE), lambda g, b: (b, 0, g))],
                  out_shape=[_sds((T, POOL_W), MX), _sds((POOL_G, LANE, LANE), F32), _sds((nb, 1, POOL_W), F32)],
                  )(dscaled, mixed, pooled, pool_grp, pool_scale)


def _lane_masks(shape):
    lane = lax.broadcasted_iota(jnp.int32, shape, len(shape) - 1)
    m_n = lane < NOPE
    m_r = jnp.logical_and(lane >= KR_LANE, lane < KR_LANE + ROPE)
    first_half = lane < KR_LANE + ROPE // 2
    return m_n, m_r, first_half


def _rot(y, first_half):
    return jnp.where(first_half, -pltpu.roll(y, LANE - ROPE // 2, 1), pltpu.roll(y, ROPE // 2, 1))


def _rot_t(v, first_half, m_r):
    return jnp.where(m_r, jnp.where(first_half, pltpu.roll(v, LANE - ROPE // 2, 1), -pltpu.roll(v, ROPE // 2, 1)), 0.0)


def _mla_in_fwd(z, w_q, w_kv, qa_gain, kva_gain, cos, sin, q_gain, k_gain, *, name, tm=256):
    T = z.shape[0]
    slab = pl.BlockSpec((tm, LANE), lambda i: (i, 0))
    vec = pl.BlockSpec((1, LANE), lambda i: (0, 0))
    whole = lambda a: pl.BlockSpec(a.shape, lambda i: (0, 0))

    def body(z_ref, wq_ref, wkv_ref, qa_ref, kva_ref, cos_ref, sin_ref, qg_ref, kg_ref,
             qn_ref, kvn_ref, qp_ref, kvp_ref, q_ref, k_ref, v_ref):
        ql = z_ref[:, Z_QL:Z_QL + QL].astype(F32)
        kvl = z_ref[:, Z_KV:Z_KV + KVL].astype(F32)
        qn = (ql * _rsq(ql) * qa_ref[...]).astype(MX)
        kvn = (kvl * _rsq(kvl) * kva_ref[...]).astype(MX)
        qn_ref[...] = qn
        kvn_ref[...] = kvn
        qp = jnp.dot(qn, wq_ref[...], preferred_element_type=F32)
        kvp = jnp.dot(kvn, wkv_ref[...], preferred_element_type=F32)
        qp_ref[...] = qp
        kvp_ref[...] = kvp
        m_n, m_r, first_half = _lane_masks((tm, LANE))
        c = cos_ref[...]
        s = sin_ref[...]
        qg = qg_ref[...]
        kg = kg_ref[...]
        xr = z_ref[:, Z_KR:Z_KR + LANE].astype(F32)
        rr = lax.rsqrt(jnp.sum(xr * xr, axis=-1, keepdims=True) * (1.0 / ROPE) + EPS)
        yr = xr * rr * kg
        kr = jnp.where(m_r, yr * c + _rot(yr, first_half) * s, 0.0)
        for h in range(NH):
            x = qp[:, h * LANE:(h + 1) * LANE]
            x2 = x * x
            rn = lax.rsqrt(jnp.sum(jnp.where(m_n, x2, 0.0), axis=-1, keepdims=True) * (1.0 / NOPE) + EPS)
            rq = lax.rsqrt(jnp.sum(jnp.where(m_r, x2, 0.0), axis=-1, keepdims=True) * (1.0 / ROPE) + EPS)
            y = x * jnp.where(m_n, rn, jnp.where(m_r, rq, 0.0)) * qg
            q_ref[:, h * LANE:(h + 1) * LANE] = ((y * c + _rot(y, first_half) * s) * ATTN_SCALE).astype(q_ref.dtype)
            xk = kvp[:, h * LANE:(h + 1) * LANE]
            rk = lax.rsqrt(jnp.sum(jnp.where(m_n, xk * xk, 0.0), axis=-1, keepdims=True) * (1.0 / NOPE) + EPS)
            k_ref[:, h * LANE:(h + 1) * LANE] = (jnp.where(m_n, xk * rk * kg, 0.0) + kr).astype(k_ref.dtype)
        v_ref[...] = kvp[:, NH * LANE:].astype(v_ref.dtype)

    return _pcall(body, name=name, grid=(T // tm,),
                  in_specs=[_row_spec(tm, LAT_W), whole(w_q), whole(w_kv), whole(qa_gain), whole(kva_gain), slab, slab, vec, vec],
                  out_specs=[_row_spec(tm, QL), _row_spec(tm, KVL), _row_spec(tm, NH * LANE), _row_spec(tm, NH * LANE + NH * NOPE),
                             _row_spec(tm, NH * LANE), _row_spec(tm, NH * LANE), _row_spec(tm, NH * NOPE)],
                  out_shape=[_sds((T, QL), MX), _sds((T, KVL), MX), _sds((T, NH * LANE), F32), _sds((T, NH * LANE + NH * NOPE), F32),
                             _sds((T, NH * LANE), MX), _sds((T, NH * LANE), MX), _sds((T, NH * NOPE), MX)],
                  )(z, w_q, w_kv, qa_gain, kva_gain, cos, sin, q_gain, k_gain)


def _mla_in_bwd(dq, dk, dv, qp, kvp, z, w_q, w_kv, qa_gain, kva_gain, cos, sin, q_gain, k_gain, *, nb, name, tm=256):
    T = dq.shape[0]
    tps = (T // nb) // tm
    slab = pl.BlockSpec((tm, LANE), lambda i: (i, 0))
    vec = pl.BlockSpec((1, LANE), lambda i: (0, 0))
    whole = lambda a: pl.BlockSpec(a.shape, lambda i: (0, 0))

    def latent_bwd(x, dy, gain):
        r = _rsq(x)
        xhat = x * r
        dxhat = dy * gain
        return r * (dxhat - xhat * jnp.mean(dxhat * xhat, axis=-1, keepdims=True)), jnp.sum(dy * xhat, axis=0, keepdims=True)

    def body(dq_ref, dk_ref, dv_ref, qp_ref, kvp_ref, z_ref, wq_ref, wkv_ref, qa_ref, kva_ref, cos_ref, sin_ref, qg_ref, kg_ref,
             dqp_ref, dkvp_ref, dkr_ref, dql_ref, dkvl_ref, st_ref, sq_ref, sk_ref):
        i = pl.program_id(0)
        qp, kvp = qp_ref, kvp_ref
        m_n, m_r, first_half = _lane_masks((tm, LANE))
        c = cos_ref[...]
        s = sin_ref[...]
        qg = qg_ref[...]
        kg = kg_ref[...]
        dqg = jnp.zeros((1, LANE), F32)
        dkg = jnp.zeros((1, LANE), F32)
        dkr_sum = jnp.zeros((tm, LANE), F32)
        for h in range(NH):
            x = qp[:, h * LANE:(h + 1) * LANE]
            x2 = x * x
            rn = lax.rsqrt(jnp.sum(jnp.where(m_n, x2, 0.0), axis=-1, keepdims=True) * (1.0 / NOPE) + EPS)
            rq = lax.rsqrt(jnp.sum(jnp.where(m_r, x2, 0.0), axis=-1, keepdims=True) * (1.0 / ROPE) + EPS)
            rfac = jnp.where(m_n, rn, jnp.where(m_r, rq, 0.0))
            xhat = x * rfac
            do = dq_ref[:, h * LANE:(h + 1) * LANE] * ATTN_SCALE
            dy = do * c + _rot_t(do * s, first_half, m_r)
            dqg = dqg + jnp.sum(dy * xhat, axis=0, keepdims=True)
            dxhat = dy * qg
            t = dxhat * xhat
            mean_n = jnp.sum(jnp.where(m_n, t, 0.0), axis=-1, keepdims=True) * (1.0 / NOPE)
            mean_r = jnp.sum(jnp.where(m_r, t, 0.0), axis=-1, keepdims=True) * (1.0 / ROPE)
            dqp_ref[:, h * LANE:(h + 1) * LANE] = (
                rfac * (dxhat - xhat * jnp.where(m_n, mean_n, jnp.where(m_r, mean_r, 0.0)))).astype(dqp_ref.dtype)

            xk = kvp[:, h * LANE:(h + 1) * LANE]
            rk = lax.rsqrt(jnp.sum(jnp.where(m_n, xk * xk, 0.0), axis=-1, keepdims=True) * (1.0 / NOPE) + EPS)
            khat = jnp.where(m_n, xk * rk, 0.0)
            dko = dk_ref[:, h * LANE:(h + 1) * LANE]
            dkn = jnp.where(m_n, dko, 0.0)
            dkg = dkg + jnp.sum(dkn * khat, axis=0, keepdims=True)
            dkhat = dkn * kg
            mean_k = jnp.sum(dkhat * khat, axis=-1, keepdims=True) * (1.0 / NOPE)
            dkvp_ref[:, h * LANE:(h + 1) * LANE] = jnp.where(m_n, rk * (dkhat - khat * mean_k), 0.0).astype(dkvp_ref.dtype)
            dkr_sum = dkr_sum + jnp.where(m_r, dko, 0.0)
        dkvp_ref[:, NH * LANE:] = dv_ref[...].astype(dkvp_ref.dtype)

        xr = z_ref[:, Z_KR:Z_KR + LANE].astype(F32)
        rr = lax.rsqrt(jnp.sum(xr * xr, axis=-1, keepdims=True) * (1.0 / ROPE) + EPS)
        rhat = xr * rr
        dyr = dkr_sum * c + _rot_t(dkr_sum * s, first_half, m_r)
        dkg = dkg + jnp.sum(dyr * rhat, axis=0, keepdims=True)
        drhat = dyr * kg
        mean_kr = jnp.sum(drhat * rhat, axis=-1, keepdims=True) * (1.0 / ROPE)
        dkr_ref[...] = jnp.where(m_r, rr * (drhat - rhat * mean_kr), 0.0).astype(dkr_ref.dtype)

        nt = (((1,), (1,)), ((), ()))
        dqn = lax.dot_general(dqp_ref[...], wq_ref[...], nt, preferred_element_type=F32)
        dkvn = lax.dot_general(dkvp_ref[...], wkv_ref[...], nt, preferred_element_type=F32)
        dql, dqa = latent_bwd(z_ref[:, Z_QL:Z_QL + QL].astype(F32), dqn, qa_ref[...])
        dkvl, dkva = latent_bwd(z_ref[:, Z_KV:Z_KV + KVL].astype(F32), dkvn, kva_ref[...])
        dql_ref[...] = dql.astype(dql_ref.dtype)
        dkvl_ref[...] = dkvl.astype(dkvl_ref.dtype)

        @pl.when(i % tps == 0)
        def _():
            st_ref[...] = jnp.zeros_like(st_ref)
            sq_ref[...] = jnp.zeros_like(sq_ref)
            sk_ref[...] = jnp.zeros_like(sk_ref)

        st_ref[0, 0:1, :] += dqg
        st_ref[0, 1:2, :] += dkg
        sq_ref[0, 0:1, :] += dqa
        sk_ref[0, 0:1, :] += dkva

    stat = lambda w: pl.BlockSpec((1, SUBLANE, w), lambda i: (i // tps, 0, 0))
    return _pcall(body, name=name, grid=(T // tm,),
                  in_specs=[_row_spec(tm, NH * LANE), _row_spec(tm, NH * LANE), _row_spec(tm, NH * NOPE),
                            _row_spec(tm, NH * LANE), _row_spec(tm, NH * LANE + NH * NOPE), _row_spec(tm, LAT_W),
                            whole(w_q), whole(w_kv), whole(qa_gain), whole(kva_gain), slab, slab, vec, vec],
                  out_specs=[_row_spec(tm, NH * LANE), _row_spec(tm, NH * LANE + NH * NOPE), slab, _row_spec(tm, QL),
                             _row_spec(tm, KVL), stat(LANE), stat(QL), stat(KVL)],
                  out_shape=[_sds((T, NH * LANE), MX), _sds((T, NH * LANE + NH * NOPE), MX), _sds((T, LANE), MX),
                             _sds((T, QL), MX), _sds((T, KVL), MX), _sds((nb, SUBLANE, LANE), F32),
                             _sds((nb, SUBLANE, QL), F32), _sds((nb, SUBLANE, KVL), F32)],
                  )(dq, dk, dv, qp, kvp, z, w_q, w_kv, qa_gain, kva_gain, cos, sin, q_gain, k_gain)


def _lower_triangle(t, keys_first=False):
    key_axis = 0 if keys_first else 1
    return lax.broadcasted_iota(jnp.int32, (t, t), key_axis) <= lax.broadcasted_iota(jnp.int32, (t, t), 1 - key_axis)


def _attn_fwd(q, k, v, *, nb, name, tq=512, comm=None):
    T = q.shape[0]
    S = T // nb
    tq = _pick(S, tq)
    nq = S // tq
    tk = tq
    npair = NH // 2

    def body(q_ref, k_ref, v_ref, o_ref, lse_ref):
        qi = pl.program_id(2)
        lane = lax.broadcasted_iota(jnp.int32, (tq, LANE), 1)
        qs = [q_ref[:, hh * LANE:(hh + 1) * LANE] for hh in range(2)]

        def block(j, carry, diagonal):
            k0 = pl.multiple_of(j * tk, tk)
            vb = v_ref[pl.ds(k0, tk), :]
            vbs = [jnp.where(lane < NOPE, vb, jnp.ones_like(vb)), jnp.where(lane < NOPE, jnp.ones_like(vb), vb)]
            new = []
            for hh in range(2):
                m, acc = carry[hh]
                kb = k_ref[pl.ds(k0, tk), hh * LANE:(hh + 1) * LANE]
                s = lax.dot_general(qs[hh], kb, (((1,), (1,)), ((), ())), preferred_element_type=F32)
                if diagonal:
                    s = jnp.where(_lower_triangle(tq), s, NEG)
                m_new = jnp.maximum(m, jnp.max(s, axis=-1, keepdims=True))
                p = jnp.exp((s - m_new).astype(MX))
                acc = jnp.exp(m - m_new) * acc + jnp.dot(p, vbs[hh], preferred_element_type=F32)
                new.append((m_new, acc))
            return tuple(new)

        init = tuple((jnp.full((tq, 1), NEG, F32), jnp.zeros((tq, LANE), F32)) for _ in range(2))
        carry = lax.fori_loop(0, qi, lambda j, c: block(j, c, False), init)
        (m0, acc0), (m1, acc1) = block(qi, carry, True)
        l0, l1 = acc0[:, NOPE:NOPE + 1], acc1[:, 0:1]
        o_ref[...] = jnp.where(lane < NOPE, acc0 / l0, acc1 / l1).astype(o_ref.dtype)
        lse_ref[...] = jnp.where(lane < NOPE, m0 + jnp.log(l0), m1 + jnp.log(l1))

    return _pcall(body, name=name, grid=(nb, npair, nq),
                  in_specs=[pl.BlockSpec((tq, 2 * LANE), lambda b, p, i: (b * nq + i, p)),
                            pl.BlockSpec((S, 2 * LANE), lambda b, p, i: (b, p)),
                            pl.BlockSpec((S, LANE), lambda b, p, i: (b, p))],
                  out_specs=[pl.BlockSpec((tq, LANE), lambda b, p, i: (b * nq + i, p)),
                             pl.BlockSpec((tq, LANE), lambda b, p, i: (b * nq + i, p))],
                  out_shape=[_sds((T, NH * NOPE), MX), _sds((T, NH * NOPE), F32)], comm=comm)(q, k, v)


def _attn_bwd(q, k, v, o, lse, do, *, nb, name, tq=512, comm=None):
    T = q.shape[0]
    S = T // nb
    tq = _pick(S, tq)
    nq = S // tq
    tk = tq
    npair = NH // 2

    def body(q_ref, k_ref, v_ref, o_ref, lse_ref, do_ref, dq_ref, dk_ref, dv_ref, lse_t, delta_t):
        lane = lax.broadcasted_iota(jnp.int32, (tq, LANE), 1)
        first = lane < NOPE
        dq_ref[...] = jnp.zeros_like(dq_ref)

        def rows_step(qi, _):
            q0 = pl.multiple_of(qi * tq, tq)
            prod = do_ref[pl.ds(q0, tq), :] * o_ref[pl.ds(q0, tq), :].astype(F32)
            d0 = jnp.sum(jnp.where(first, prod, 0.0), axis=-1, keepdims=True)
            d1 = jnp.sum(jnp.where(first, 0.0, prod), axis=-1, keepdims=True)
            delta_t[:, pl.ds(q0, tq)] = jnp.where(first, d0, d1).T
            lse_t[:, pl.ds(q0, tq)] = lse_ref[pl.ds(q0, tq), :].T
            return 0

        lax.fori_loop(0, nq, rows_step, 0)

        def kv_step(kj, _):
            k0 = pl.multiple_of(kj * tk, tk)
            kbs = [k_ref[pl.ds(k0, tk), hh * LANE:(hh + 1) * LANE] for hh in range(2)]
            vb = v_ref[pl.ds(k0, tk), :]

            def q_block(qi, carry, diagonal):
                dk0, dk1, dv = carry
                dks = [dk0, dk1]
                q0 = pl.multiple_of(qi * tq, tq)
                dov = do_ref[pl.ds(q0, tq), :]
                for hh in range(2):
                    qh = q_ref[pl.ds(q0, tq), hh * LANE:(hh + 1) * LANE]
                    dob = jnp.where(first if hh == 0 else jnp.logical_not(first), dov, 0.0).astype(MX)
                    st = lax.dot_general(kbs[hh], qh, (((1,), (1,)), ((), ())), preferred_element_type=F32)
                    pt = jnp.exp((st - lse_t[hh * NOPE:hh * NOPE + 1, pl.ds(q0, tq)]).astype(MX))
                    if diagonal:
                        pt = jnp.where(_lower_triangle(tq, keys_first=True), pt, jnp.zeros_like(pt))
                    dpt = lax.dot_general(vb, dob, (((1,), (1,)), ((), ())), preferred_element_type=F32)
                    dst = pt * (dpt - delta_t[hh * NOPE:hh * NOPE + 1, pl.ds(q0, tq)]).astype(MX)
                    dks[hh] = dks[hh] + jnp.dot(dst, qh, preferred_element_type=F32)
                    dv = dv + jnp.dot(pt, dob, preferred_element_type=F32)
                    dq_ref[pl.ds(q0, tq), hh * LANE:(hh + 1) * LANE] += lax.dot_general(
                        dst, kbs[hh], (((0,), (0,)), ((), ())), preferred_element_type=F32)
                return dks[0], dks[1], dv

            zero = jnp.zeros((tk, LANE), F32)
            carry = q_block(kj, (zero, zero, zero), True)
            dk0, dk1, dv = lax.fori_loop(kj + 1, nq, lambda qi, c: q_block(qi, c, False), carry)
            dk_ref[pl.ds(k0, tk), 0:LANE] = dk0
            dk_ref[pl.ds(k0, tk), LANE:2 * LANE] = dk1
            dv_ref[pl.ds(k0, tk), :] = dv
            return 0

        lax.fori_loop(0, nq, kv_step, 0)

    pair256 = pl.BlockSpec((S, 2 * LANE), lambda b, p: (b, p))
    pair128 = pl.BlockSpec((S, LANE), lambda b, p: (b, p))
    return _pcall(body, name=name, grid=(nb, npair),
                  in_specs=[pair256, pair256, pair128, pair128, pair128, pair128],
                  out_specs=[pair256, pair256, pair128],
                  out_shape=[_sds((T, NH * LANE), F32), _sds((T, NH * LANE), F32), _sds((T, NH * NOPE), F32)],
                  scratch=[pltpu.VMEM((LANE, S), F32), pltpu.VMEM((LANE, S), F32)], comm=comm)(q, k, v, o, lse, do)


def _run(plan, fn, *args, name, **kw):
    rider = plan.rider(name)
    if rider is None:
        return fn(*args, name=name, **kw)
    outs, landed = fn(*args, name=name, comm=rider, **kw)
    plan.landed(name, landed)
    return outs


def _layer_fwd_bwd(x, tgt, mod3, cos, sin, plan, P):
    nb = mod3.shape[0]
    W = plan.W
    h1, gu1, a1 = _run(plan, _ffn_up_first, x, mod3, P["norm_ffn1"], W["ffn1_in"], sub=0, name="ffn1_up_a")
    gu1, a1 = _run(plan, _ffn_up_second, h1, W["ffn1_in"], gu1, a1, name="ffn1_up_b")
    f1, x1, h2 = _run(plan, _mm_resid_norm, a1, W["ffn1_out"], x, mod3, P["norm_mix"], sub=1, coef=0.5, name="ffn1_out")
    z = _run(plan, _mm, h2, W["w_in"], mode="nt", out_dtype=MX, name="mix_in", tn=1664)
    pooled, mixed, scaled = _pool_fwd(z, W["pool_grp"], P["pool_scale"], nb=nb, name="pool_fwd")
    br_pool = _mm(scaled, W["pool_proj"], mode="nn", out_dtype=MX, name="pool_proj")
    qn, kvn, qp, kvp, q, k, v = _mla_in_fwd(z, W["q_up"], W["kv_up"], P["q_a_norm"], P["kv_a_norm"], cos, sin,
                                            P["q_gain"], P["k_gain"], name="mla_in_fwd")
    attn, lse = _run(plan, _attn_fwd, q, k, v, nb=nb, name="attn_fwd")
    br_mla, merged, mo, x2, h3 = _mix_out_fwd(z, br_pool, attn, W["mla_proj"], W["w_out"], x1, mod3, P["norm_ffn2"], sub=2,
                                              name="mix_out", tm=256)
    gu2, a2 = _ffn_up(h3, W["ffn2_in"], name="ffn2_up")
    dy, df2, st_fin, loss = _mm_loss(a2, W["ffn2_out"], x2, tgt, mod3, name="ffn2_out_loss")

    plan.grad("ffn2_out", _dw(a2, df2, name="d_ffn2_out"))
    dgu2 = _ffn_dact(df2, gu2, W["ffn2_out"], name="ffn2_dact")
    plan.grad("ffn2_in", _ffn_dw_in(h3, dgu2, name="d_ffn2_in"))
    dx2, dmo, st3 = _run(plan, _ffn_dh_norm, dgu2, W["ffn2_in"], x2, dy, mo, mod3, P["norm_ffn2"], sub=2, coef=1.0,
                         name="d_ffn2_h")
    plan.grad("w_out", _dw(merged, dmo, name="d_mix_out"))
    dbr_pool, dbr_mla, dgates, dattn = _mix_out_bwd(dmo, W["w_out"], z, br_pool, br_mla, W["mla_proj"], name="mix_out_bwd", tm=256)
    plan.grad("pool_proj", _dw(scaled, dbr_pool, name="d_pool_proj"))
    dscaled = _mm(dbr_pool, W["pool_proj"], mode="nt", out_dtype=F32, name="d_pool_scaled")
    du_pool, d_pool_grp, d_pool_scale = _pool_bwd(dscaled, mixed, pooled, W["pool_grp"], P["pool_scale"], nb=nb, name="pool_bwd")
    plan.grad("mla_proj", _dw(attn, dbr_mla, name="d_mla_proj"))
    dq, dk, dv = _run(plan, _attn_bwd, q, k, v, attn, lse, dattn, nb=nb, name="attn_bwd")
    dqp, dkvp, dkr, dql, dkvl, st_prep, st_q, st_kv = _mla_in_bwd(
        dq, dk, dv, qp, kvp, z, W["q_up"], W["kv_up"], P["q_a_norm"], P["kv_a_norm"], cos, sin, P["q_gain"], P["k_gain"],
        nb=nb, name="mla_in_bwd")
    plan.grad("q_up", _dw(qn, dqp, name="d_q_up"))
    plan.grad("kv_up", _dw(kvn, dkvp, name="d_kv_up"))
    dz = jnp.concatenate([du_pool, dql, dkvl, dkr, dgates], axis=1)
    plan.grad("w_in", _run(plan, _dw, h2, dz, name="d_mix_in", tm=256, tn=1664, out_t=True))
    dx1, df1, st2 = _run(plan, _mm_norm_bwd, dz, W["w_in"], x1, dx2, f1, mod3, P["norm_mix"], sub=1, coef=0.5, name="d_mix_h")
    plan.grad("ffn1_out", _run(plan, _dw, a1, df1, name="d_ffn1_out"))
    dgu1 = _run(plan, _ffn_dact, df1, gu1, W["ffn1_out"], name="ffn1_dact")
    half = x.shape[1] // 2
    plan.grad("ffn1_in@0", _run(plan, _ffn_dw_in, h1, dgu1, rows=(0, half), name="d_ffn1_in_a"))
    plan.grad("ffn1_in@1", _run(plan, _ffn_dw_in, h1, dgu1, rows=(half, half), name="d_ffn1_in_b"))
    dh1 = _run(plan, _ffn_dh, dgu1, W["ffn1_in"], name="d_ffn1_h")
    grad_x, st1 = _run(plan, _bwd_block, x, dh1, dx1, None, mod3, P["norm_ffn1"], sub=0, coef=0.0, name="bwd_norm1")

    return loss, grad_x, (st1, st2, st3, st_fin, st_q, st_kv, st_prep, d_pool_scale), d_pool_grp


def _padded(rows):
    return rows + -rows % (4 * SUBLANE)


def _pad_rows(a):
    pad = [(0, 0)] * a.ndim
    pad[-2] = (0, _padded(a.shape[-2]) - a.shape[-2])
    return jnp.pad(a, pad)


def _w_in_pieces(shard):
    runs = []
    for ref0, ref1, k0 in ((0, Z_KR, 0), (Z_KR, Z_KR + ROPE, Z_KR + KR_LANE), (Z_KR + ROPE, N_CHIP * shard, Z_GP)):
        for q in range(N_CHIP):
            lo, hi = max(ref0, q * shard), min(ref1, (q + 1) * shard)
            if lo < hi:
                runs.append((k0 + lo - ref0, q * _padded(shard) + lo - q * shard, hi - lo))
    return runs


def _w_in_stacked_to_kernel(st, shard):
    flat = st.reshape(-1, st.shape[2])
    parts, at = [], 0
    for k0, s0, n in sorted(_w_in_pieces(shard)):
        parts += [jnp.zeros((k0 - at, flat.shape[1]), st.dtype)] * (k0 > at) + [flat[s0:s0 + n]]
        at = k0 + n
    assert at == Z_W
    return jnp.concatenate(parts, axis=0)


def _w_in_kernel_to_stacked(gt, shard):
    parts, at = [], 0
    for s0, k0, n in sorted((s, k, n) for k, s, n in _w_in_pieces(shard)) + [(N_CHIP * _padded(shard), 0, 0)]:
        parts += [jnp.zeros((s0 - at, gt.shape[1]), gt.dtype)] * (s0 > at) + [gt[k0:k0 + n]] * (n > 0)
        at = s0 + n
    return jnp.concatenate(parts, axis=0).reshape(N_CHIP, _padded(shard), gt.shape[1])


def _q_up_to_kernel(w):
    k = w.shape[0]
    return jnp.pad(w.reshape(k, NH, NOPE + ROPE), ((0, 0), (0, 0), (0, LANE - NOPE - ROPE))).reshape(k, NH * LANE)


def _q_up_from_kernel(g):
    k = g.shape[0]
    return g.reshape(k, NH, LANE)[:, :, :NOPE + ROPE].reshape(k, NH * (NOPE + ROPE))


def _kv_up_to_kernel(w):
    k = w.shape[0]
    w3 = w.reshape(k, NH, 2 * NOPE)
    kpart = jnp.pad(w3[:, :, :NOPE], ((0, 0), (0, 0), (0, LANE - NOPE))).reshape(k, NH * LANE)
    return jnp.concatenate([kpart, w3[:, :, NOPE:].reshape(k, NH * NOPE)], axis=1)


def _kv_up_from_kernel(g):
    k = g.shape[0]
    kpart = g[:, :NH * LANE].reshape(k, NH, LANE)[:, :, :NOPE]
    vpart = g[:, NH * LANE:].reshape(k, NH, NOPE)
    return jnp.concatenate([kpart, vpart], axis=2).reshape(k, NH * 2 * NOPE)


def _gain_slab(nope, rope):
    return jnp.concatenate([nope, rope, jnp.zeros((1, LANE - NOPE - ROPE), nope.dtype)], axis=1)


def _rope_tables(positions):
    inv_freq = 10000.0 ** (-jnp.arange(0, ROPE, 2, dtype=F32) / ROPE)
    ang = positions.astype(F32)[:, None] * inv_freq
    ang = jnp.concatenate([ang, ang], axis=-1)
    t = positions.shape[0]
    cos = jnp.concatenate([jnp.ones((t, KR_LANE), F32), jnp.cos(ang), jnp.ones((t, LANE - KR_LANE - ROPE), F32)], axis=1)
    sin = jnp.concatenate([jnp.zeros((t, KR_LANE), F32), jnp.sin(ang), jnp.zeros((t, LANE - KR_LANE - ROPE), F32)], axis=1)
    return cos, sin


def _coords():
    return lax.axis_index("x"), lax.axis_index("y"), lax.axis_index("c")


HBM_SPEC = pl.BlockSpec(memory_space=pl.ANY)
VMEM_SPEC = pl.BlockSpec(memory_space=pltpu.VMEM)


CHIP_RELS = ((1, 0), (0, 1), (1, 1))


def _gather_comm(shards):
    n = len(shards)

    def ici(ins, outs, sems, a, j, x, y, cc):
        half = ins[a].shape[0] // 2
        mine = pl.ds(cc * half, half)
        dx, dy = CHIP_RELS[j]
        return pltpu.make_async_remote_copy(src_ref=ins[a].at[mine], dst_ref=outs[a].at[2 * x + y, mine],
                                            send_sem=sems[0].at[a, j], recv_sem=sems[1].at[a, j],
                                            device_id=(x ^ dx, y ^ dy, cc), device_id_type=MESH)

    def d2d(ins, outs, sems, a, j, x, y, cc, half_of):
        half = ins[a].shape[0] // 2
        dx, dy = CHIP_RELS[j]
        landed = outs[a].at[2 * (x ^ dx) + (y ^ dy), pl.ds(half_of * half, half)]
        return pltpu.make_async_remote_copy(src_ref=landed, dst_ref=landed, send_sem=sems[2].at[a, j], recv_sem=sems[3].at[a, j],
                                            device_id=(x, y, 1 - cc), device_id_type=MESH)

    def own(ins, outs, sems, a, x, y):
        return pltpu.make_async_copy(ins[a], outs[a].at[2 * x + y], sems[4].at[a])

    def start(ins, outs, sems):
        x, y, cc = _coords()
        for a in range(n):
            own(ins, outs, sems, a, x, y).start()
            for j in range(3):
                ici(ins, outs, sems, a, j, x, y, cc).start()

    def finish(ins, outs, sems):
        x, y, cc = _coords()
        for a in range(n):
            for j in range(3):
                ici(ins, outs, sems, a, j, x, y, cc).wait_recv()
                d2d(ins, outs, sems, a, j, x, y, cc, cc).start()
        for a in range(n):
            for j in range(3):
                d2d(ins, outs, sems, a, j, x, y, cc, 1 - cc).wait_recv()
        for a in range(n):
            for j in range(3):
                ici(ins, outs, sems, a, j, x, y, cc).wait_send()
                d2d(ins, outs, sems, a, j, x, y, cc, cc).wait_send()
            own(ins, outs, sems, a, x, y).wait()

    dma = pltpu.SemaphoreType.DMA
    return _Comm(shards, [_sds((N_CHIP,) + s.shape, s.dtype) for s in shards],
                 [dma((n, 3)), dma((n, 3)), dma((n, 3)), dma((n, 3)), dma((n,))], start, finish)


def _swap_comm(parts):
    n = len(parts)

    def copy(ins, outs, sems, a):
        x, y, cc = _coords()
        half = ins[a].shape[1] // 2
        return pltpu.make_async_remote_copy(src_ref=ins[a].at[:, pl.ds((1 - cc) * half, half)], dst_ref=outs[a],
                                            send_sem=sems[0].at[a], recv_sem=sems[1].at[a], device_id=(x, y, 1 - cc),
                                            device_id_type=MESH)

    def start(ins, outs, sems):
        for a in range(n):
            copy(ins, outs, sems, a).start()

    def finish(ins, outs, sems):
        for a in range(n):
            copy(ins, outs, sems, a).wait()

    dma = pltpu.SemaphoreType.DMA
    return _Comm(parts, [_sds((p.shape[0], p.shape[1] // 2, p.shape[2]), p.dtype) for p in parts], [dma((n,)), dma((n,))],
                 start, finish)


def _add_half(full, other, cidx, *, name):
    nch, r, c = full.shape
    half = r // 2
    tr = _pick_rows(half)
    nbk = half // tr
    grid_spec = pltpu.PrefetchScalarGridSpec(
        num_scalar_prefetch=1, grid=(nch, nbk),
        in_specs=[pl.BlockSpec((1, tr, c), lambda j, i, cref: (j, cref[0] * nbk + i, 0)),
                  pl.BlockSpec((1, tr, c), lambda j, i, cref: (j, i, 0))],
        out_specs=pl.BlockSpec((1, tr, c), lambda j, i, cref: (j, i, 0)))

    def body(cref, a_ref, b_ref, o_ref):
        o_ref[...] = (a_ref[...] + b_ref[...]).astype(o_ref.dtype)

    return _pcall(body, name=name, out_shape=_sds((nch, half, c), MX), grid_spec=grid_spec)(cidx, full, other)


def _pick_rows(rows, target=512):
    best = None
    for t in range(16, min(rows, target) + 1, 16):
        if rows % t == 0:
            best = t
    return rows if best is None else best


def _exchange_comm(parts):
    n = len(parts)

    def send(ins, outs, sems, a, j, x, y, cc):
        dx, dy = CHIP_RELS[j]
        return pltpu.make_async_remote_copy(src_ref=ins[a].at[2 * (x ^ dx) + (y ^ dy)], dst_ref=outs[a].at[2 * x + y],
                                            send_sem=sems[0].at[a, j], recv_sem=sems[1].at[a, j],
                                            device_id=(x ^ dx, y ^ dy, cc), device_id_type=MESH)

    def landing(ins, outs, sems, a, j, x, y, cc):
        dx, dy = CHIP_RELS[j]
        peer_chip = 2 * (x ^ dx) + (y ^ dy)
        return pltpu.make_async_remote_copy(src_ref=ins[a].at[peer_chip], dst_ref=outs[a].at[peer_chip], send_sem=sems[0].at[a, j],
                                            recv_sem=sems[1].at[a, j], device_id=(x, y, cc), device_id_type=MESH)

    def own(ins, outs, sems, a, x, y):
        return pltpu.make_async_copy(ins[a].at[2 * x + y], outs[a].at[2 * x + y], sems[2].at[a])

    def start(ins, outs, sems):
        x, y, cc = _coords()
        for a in range(n):
            own(ins, outs, sems, a, x, y).start()
            for j in range(3):
                send(ins, outs, sems, a, j, x, y, cc).start()

    def finish(ins, outs, sems):
        x, y, cc = _coords()
        for a in range(n):
            for j in range(3):
                landing(ins, outs, sems, a, j, x, y, cc).wait_recv()
        for a in range(n):
            for j in range(3):
                send(ins, outs, sems, a, j, x, y, cc).wait_send()
            own(ins, outs, sems, a, x, y).wait()

    dma = pltpu.SemaphoreType.DMA
    return _Comm(parts, [_sds(p.shape, p.dtype) for p in parts], [dma((n, 3)), dma((n, 3)), dma((n,))], start, finish)


def _sum_chips(q, cidx, *, name):
    nch, h, c = q.shape
    tr = _pick_rows(h)
    nbk = h // tr
    grid_spec = pltpu.PrefetchScalarGridSpec(
        num_scalar_prefetch=1, grid=(nbk,),
        in_specs=[pl.BlockSpec((nch, tr, c), lambda i, cref: (0, i, 0))],
        out_specs=pl.BlockSpec((tr, c), lambda i, cref: (cref[0] * nbk + i, 0)))

    def body(cref, q_ref, o_ref):
        acc = q_ref[0].astype(F32) + q_ref[1].astype(F32)
        acc = acc + q_ref[2].astype(F32)
        o_ref[...] = acc + q_ref[3].astype(F32)

    return _pcall(body, name=name, out_shape=_sds((2 * h, c), F32), grid_spec=grid_spec)(cidx, q)


def _join_comm(fulls):
    n = len(fulls)

    def half_copy(outs, sems, a, which):
        x, y, cc = _coords()
        h = outs[a].shape[0] // 2
        rows = outs[a].at[pl.ds((cc if which == 0 else 1 - cc) * h, h)]
        return pltpu.make_async_remote_copy(src_ref=rows, dst_ref=rows, send_sem=sems[0].at[a], recv_sem=sems[1].at[a],
                                            device_id=(x, y, 1 - cc), device_id_type=MESH)

    def start(ins, outs, sems):
        for a in range(n):
            half_copy(outs, sems, a, 0).start()

    def finish(ins, outs, sems):
        for a in range(n):
            half_copy(outs, sems, a, 1).wait_recv()
        for a in range(n):
            half_copy(outs, sems, a, 0).wait_send()

    dma = pltpu.SemaphoreType.DMA
    return _Comm(fulls, [_sds(p.shape, p.dtype) for p in fulls], [dma((n,)), dma((n,))], start, finish,
                 aliases={a: a for a in range(n)})


def _ada_prologue(c, w, b, *, name, comm=None):
    nb, dm = c.shape
    n = w.shape[1]

    def body(c_ref, w_ref, b_ref, call_ref, land_ref, cpad, mod_s, g_send, g_recv, m_send, m_recv):
        x, y, cc = _coords()
        me = 4 * x + 2 * y + cc
        chip = 2 * x + y
        cpad[...] = jnp.zeros_like(cpad)
        cpad[0:nb, :] = c_ref[...]
        copies = []
        for kk in range(1, N_DEV):
            peer = (x ^ (kk >> 2), y ^ ((kk >> 1) & 1), cc ^ (kk & 1))
            cp = pltpu.make_async_remote_copy(src_ref=cpad, dst_ref=call_ref.at[me], send_sem=g_send.at[kk - 1],
                                              recv_sem=g_recv.at[kk - 1], device_id=peer, device_id_type=MESH)
            cp.start()
            copies.append(cp)
        call_ref[me] = cpad[...]
        for kk in range(1, N_DEV):
            pltpu.make_async_remote_copy(src_ref=cpad, dst_ref=call_ref.at[me ^ kk], send_sem=g_send.at[kk - 1],
                                         recv_sem=g_recv.at[kk - 1], device_id=(x, y, cc), device_id_type=MESH).wait_recv()
        cv = call_ref[...].reshape(N_DEV * SUBLANE, dm)
        act = (cv * _sigmoid(cv)).astype(MX)
        mod = jnp.dot(act, w_ref[...].astype(MX), preferred_element_type=F32) + b_ref[...]
        mod_s[...] = mod.reshape(N_DEV, SUBLANE, n)
        for j, (dx, dy) in enumerate(CHIP_RELS):
            cp = pltpu.make_async_remote_copy(src_ref=mod_s.at[4 * (x ^ dx) + 2 * (y ^ dy) + cc], dst_ref=land_ref.at[chip],
                                              send_sem=m_send.at[j], recv_sem=m_recv.at[j],
                                              device_id=(x ^ dx, y ^ dy, cc), device_id_type=MESH)
            cp.start()
            copies.append(cp)
        land_ref[chip] = mod_s[me]
        for j, (dx, dy) in enumerate(CHIP_RELS):
            pltpu.make_async_remote_copy(src_ref=mod_s.at[me], dst_ref=land_ref.at[2 * (x ^ dx) + (y ^ dy)], send_sem=m_send.at[j],
                                         recv_sem=m_recv.at[j], device_id=(x, y, cc), device_id_type=MESH).wait_recv()
        for cp in copies:
            cp.wait_send()

    dma = pltpu.SemaphoreType.DMA
    return _pcall(body, name=name, in_specs=[VMEM_SPEC] * 3, out_specs=[VMEM_SPEC] * 2,
                  out_shape=[_sds((N_DEV, SUBLANE, dm), F32), _sds((N_CHIP, SUBLANE, n), F32)],
                  scratch=[pltpu.VMEM((SUBLANE, dm), F32), pltpu.VMEM((N_DEV, SUBLANE, n), F32),
                           dma((N_DEV - 1,)), dma((N_DEV - 1,)), dma((3,)), dma((3,))], comm=comm)(c, w, b)


def _ada_bwd(c_all, dmod_cols, *, name):
    m, kdim = c_all.shape
    n = dmod_cols.shape[1]
    tn = _pick(n, 1152)

    def body(c_ref, d_ref, o_ref):
        cv = c_ref[...]
        act = (cv * _sigmoid(cv)).astype(MX)
        o_ref[...] = lax.dot_general(act, d_ref[...].astype(MX), (((0,), (0,)), ((), ())), preferred_element_type=F32)

    return _pcall(body, name=name, out_shape=_sds((kdim, n), F32), grid=(n // tn,),
                  in_specs=[pl.BlockSpec((m, kdim), lambda j: (0, 0)), pl.BlockSpec((m, tn), lambda j: (0, j))],
                  out_specs=pl.BlockSpec((kdim, tn), lambda j: (0, j)))(c_all, dmod_cols)


SLAB_W = 1024
RED_ROWS = 8


LOSS_LANE = SLAB_W - LANE


def _pack_stats(st1, st2, st3, st_fin, st_q, st_kv, st_prep, d_pool_scale, loss8, *, name):
    nb = st1.shape[0]
    dm_rows = -(-9 * nb // SUBLANE) * SUBLANE

    def body(s1, s2, s3, sf, sq, skv, sp, sps, loss_ref, o_ref):
        o_ref[...] = jnp.zeros_like(o_ref)
        for s in range(nb):
            rows = [s1[s, 0:1, :], s1[s, 1:2, :], s2[s, 3:4, :], s2[s, 0:1, :], s2[s, 1:2, :], s3[s, 3:4, :], s3[s, 0:1, :],
                    s3[s, 1:2, :], sf[s, 0:1, :]]
            for k, row in enumerate(rows):
                o_ref[9 * s + k:9 * s + k + 1, :] = row

        def over_seq(ref, r):
            acc = ref[0, r:r + 1, :]
            for s in range(1, nb):
                acc = acc + ref[s, r:r + 1, :]
            return acc

        o_ref[dm_rows + 0:dm_rows + 1, :] = over_seq(s1, 2)
        o_ref[dm_rows + 1:dm_rows + 2, :] = over_seq(s2, 2)
        o_ref[dm_rows + 2:dm_rows + 3, :] = over_seq(s3, 2)
        o_ref[dm_rows + 3:dm_rows + 4, 0:POOL_W] = over_seq(sps, 0)
        o_ref[dm_rows + 4:dm_rows + 5, 0:QL] = over_seq(sq, 0)
        o_ref[dm_rows + 5:dm_rows + 6, 0:KVL] = over_seq(skv, 0)
        o_ref[dm_rows + 6:dm_rows + 7, 0:LANE] = over_seq(sp, 0)
        o_ref[dm_rows + 7:dm_rows + 8, 0:LANE] = over_seq(sp, 1)
        o_ref[dm_rows + 7:dm_rows + 8, LOSS_LANE:] = loss_ref[0:1, :]

    return _pcall(body, name=name, out_shape=_sds((dm_rows + RED_ROWS, SLAB_W), F32), in_specs=[VMEM_SPEC] * 9,
                  out_specs=VMEM_SPEC)(st1, st2, st3, st_fin, st_q, st_kv, st_prep, d_pool_scale, loss8)


def _small_allreduce(slab, pool, *, name, comm=None):
    rows, w = slab.shape
    dm = rows - RED_ROWS
    prow, pw = pool.shape
    crow = RED_ROWS + 2 * dm

    def body(slab_ref, pool_ref, red_ref, dmod_ref, ptot_ref, sib_slab, sib_pool, chip_slab, chip_pool, land_slab, land_pool,
             a_send, a_recv, b_send, b_recv):
        x, y, cc = _coords()
        chip = 2 * x + y
        sib = (x, y, 1 - cc)
        to_sib = [pltpu.make_async_remote_copy(src_ref=slab_ref, dst_ref=sib_slab, send_sem=a_send.at[0], recv_sem=a_recv.at[0],
                                               device_id=sib, device_id_type=MESH),
                  pltpu.make_async_remote_copy(src_ref=pool_ref, dst_ref=sib_pool, send_sem=a_send.at[1], recv_sem=a_recv.at[1],
                                               device_id=sib, device_id_type=MESH)]
        for cp in to_sib:
            cp.start()
        for cp in to_sib:
            cp.wait()
        mine_dm, theirs_dm = slab_ref[0:dm, :], sib_slab[0:dm, :]
        chip_slab[0:RED_ROWS, :] = slab_ref[dm:, :] + sib_slab[dm:, :]
        chip_slab[RED_ROWS:RED_ROWS + dm, :] = jnp.where(cc == 0, mine_dm, theirs_dm)
        chip_slab[RED_ROWS + dm:, :] = jnp.where(cc == 0, theirs_dm, mine_dm)
        chip_pool[...] = pool_ref[...] + sib_pool[...]

        sends = []
        for j, (dx, dy) in enumerate(CHIP_RELS):
            peer = (x ^ dx, y ^ dy, cc)
            sends.append(pltpu.make_async_remote_copy(src_ref=chip_slab, dst_ref=land_slab.at[chip], send_sem=b_send.at[j, 0],
                                                      recv_sem=b_recv.at[j, 0], device_id=peer, device_id_type=MESH))
            sends.append(pltpu.make_async_remote_copy(src_ref=chip_pool, dst_ref=land_pool.at[chip], send_sem=b_send.at[j, 1],
                                                      recv_sem=b_recv.at[j, 1], device_id=peer, device_id_type=MESH))
        for cp in sends:
            cp.start()
        land_slab[chip] = chip_slab[...]
        land_pool[chip] = chip_pool[...]
        for j, (dx, dy) in enumerate(CHIP_RELS):
            peer_chip = 2 * (x ^ dx) + (y ^ dy)
            pltpu.make_async_remote_copy(src_ref=chip_slab, dst_ref=land_slab.at[peer_chip], send_sem=b_send.at[j, 0],
                                         recv_sem=b_recv.at[j, 0], device_id=(x, y, cc), device_id_type=MESH).wait_recv()
            pltpu.make_async_remote_copy(src_ref=chip_pool, dst_ref=land_pool.at[peer_chip], send_sem=b_send.at[j, 1],
                                         recv_sem=b_recv.at[j, 1], device_id=(x, y, cc), device_id_type=MESH).wait_recv()
        red = land_slab[0, 0:RED_ROWS, :]
        ptot = land_pool[0]
        for ch in range(1, N_CHIP):
            red = red + land_slab[ch, 0:RED_ROWS, :]
            ptot = ptot + land_pool[ch]
        red_ref[...] = red
        ptot_ref[...] = ptot
        for ch in range(N_CHIP):
            dmod_ref[2 * dm * ch:2 * dm * (ch + 1), :] = land_slab[ch, RED_ROWS:, :]
        for cp in sends:
            cp.wait_send()

    dma = pltpu.SemaphoreType.DMA
    return _pcall(body, name=name, in_specs=[VMEM_SPEC, VMEM_SPEC], out_specs=[VMEM_SPEC] * 3,
                  out_shape=[_sds((RED_ROWS, w), F32), _sds((N_DEV * dm, w), F32), _sds((prow, pw), F32)],
                  scratch=[pltpu.VMEM((rows, w), F32), pltpu.VMEM((prow, pw), F32), pltpu.VMEM((crow, w), F32),
                           pltpu.VMEM((prow, pw), F32), pltpu.VMEM((N_CHIP, crow, w), F32), pltpu.VMEM((N_CHIP, prow, pw), F32),
                           dma((2,)), dma((2,)), dma((3, 2)), dma((3, 2))], comm=comm)(slab, pool)


def _adamw_math(w, g, m, v):
    mn = ADAM_B1 * m + (1.0 - ADAM_B1) * g
    vn = ADAM_B2 * v + (1.0 - ADAM_B2) * (g * g)
    bc1 = 1.0 / (1.0 - ADAM_B1 ** ADAM_STEP)
    bc2 = 1.0 / (1.0 - ADAM_B2 ** ADAM_STEP)
    return -ADAM_LR * ((mn * bc1) / (jnp.sqrt(vn * bc2) + ADAM_EPS) + ADAM_WD * w), mn, vn


def _small_update(red, dmod_all, pool_total, nb, params, *, name):
    names = list(SMALL)
    dm = dmod_all.shape[0] // N_DEV

    def grad_of(nm, red_ref, dmod_ref, ptot_ref):
        if nm == "b_ada":
            acc = None
            for d in range(N_DEV):
                for s in range(nb):
                    blk = dmod_ref[d * dm + 9 * s:d * dm + 9 * s + 9, :]
                    acc = blk if acc is None else acc + blk
            return jnp.concatenate([acc[k:k + 1, :] for k in range(9)], axis=1)
        if nm == "pool_grp":
            return ptot_ref[...]
        row, lo, n = {"norm_ffn1": (0, 0, D), "norm_mix": (1, 0, D), "norm_ffn2": (2, 0, D), "pool_scale": (3, 0, POOL_W),
                      "q_a_norm": (4, 0, QL), "kv_a_norm": (5, 0, KVL), "q_norm_nope": (6, 0, NOPE),
                      "q_norm_rope": (6, NOPE, ROPE), "k_norm_nope": (7, 0, NOPE), "k_norm_rope": (7, KR_LANE, ROPE)}[nm]
        return red_ref[row:row + 1, lo:lo + n]

    def body(*refs):
        red_ref, dmod_ref, ptot_ref = refs[:3]
        ins = refs[3:3 + 3 * len(names)]
        outs = refs[3 + 3 * len(names):]
        outs[4 * len(names)][...] = red_ref[RED_ROWS - 1:RED_ROWS, LOSS_LANE:]
        for i, nm in enumerate(names):
            g = grad_of(nm, red_ref, dmod_ref, ptot_ref)
            d, mn, vn = _adamw_math(ins[3 * i][...], g, ins[3 * i + 1][...], ins[3 * i + 2][...])
            outs[4 * i][...] = g
            outs[4 * i + 1][...] = d
            outs[4 * i + 2][...] = mn
            outs[4 * i + 3][...] = vn

    flat_in = [a for nm in names for a in params[nm]]
    out_shape = [_sds(params[nm][0].shape, F32) for nm in names for _ in range(4)] + [_sds((1, LANE), F32)]
    res = _pcall(body, name=name, in_specs=[VMEM_SPEC] * (3 + len(flat_in)), out_specs=[VMEM_SPEC] * len(out_shape),
                 out_shape=out_shape)(red, dmod_all, pool_total, *flat_in)
    return {nm: tuple(res[4 * i:4 * i + 4]) for i, nm in enumerate(names)}, res[-1]


def _adamw(w, g, m, v, *, name, comm=None):
    r, c = w.shape
    tr = _pick_rows(r, 256)
    tc = c if tr < r else _pick(c, 256)
    spec = pl.BlockSpec((tr, tc), lambda i, j: (i, j))
    bc1 = 1.0 / (1.0 - ADAM_B1 ** ADAM_STEP)
    bc2 = 1.0 / (1.0 - ADAM_B2 ** ADAM_STEP)

    def body(w_ref, g_ref, m_ref, v_ref, d_ref, mo_ref, vo_ref):
        gv = g_ref[...]
        mn = ADAM_B1 * m_ref[...] + (1.0 - ADAM_B1) * gv
        vn = ADAM_B2 * v_ref[...] + (1.0 - ADAM_B2) * (gv * gv)
        mo_ref[...] = mn
        vo_ref[...] = vn
        d_ref[...] = -ADAM_LR * ((mn * bc1) / (jnp.sqrt(vn * bc2) + ADAM_EPS) + ADAM_WD * w_ref[...])

    out = _sds((r, c), F32)
    return _pcall(body, name=name, out_shape=[out, out, out], grid=(r // tr, c // tc), in_specs=[spec] * 4, out_specs=[spec] * 3,
                  comm=comm)(w, g, m, v)


BIG = ("w_ffn1_in", "w_ffn1_out", "w_in", "w_pool_proj", "w_q_up", "w_kv_up", "w_mla_proj", "w_out", "w_ffn2_in", "w_ffn2_out")
ROW_SHARDED = ("w_ffn1_out", "w_out", "w_ffn2_out")
KERNEL_NAME = {"w_ffn1_in": "ffn1_in", "w_ffn1_out": "ffn1_out", "w_in": "w_in", "w_pool_proj": "pool_proj", "w_q_up": "q_up",
               "w_kv_up": "kv_up", "w_mla_proj": "mla_proj", "w_out": "w_out", "w_ffn2_in": "ffn2_in", "w_ffn2_out": "ffn2_out"}
WEIGHTS = ("w_ada", "b_ada", "norm_ffn1", "w_ffn1_in", "w_ffn1_out", "norm_mix", "w_in", "pool_grp", "pool_scale", "w_pool_proj",
           "q_a_norm", "w_q_up", "kv_a_norm", "w_kv_up", "q_norm_nope", "q_norm_rope", "k_norm_nope", "k_norm_rope", "w_mla_proj",
           "w_out", "norm_ffn2", "w_ffn2_in", "w_ffn2_out")
SMALL = ("b_ada", "norm_ffn1", "norm_mix", "pool_grp", "pool_scale", "q_a_norm", "kv_a_norm", "q_norm_nope", "q_norm_rope",
         "k_norm_nope", "k_norm_rope", "norm_ffn2")


def _assemble(name, stacked):
    if name in ROW_SHARDED:
        return stacked.reshape(stacked.shape[0] * stacked.shape[1], stacked.shape[2])
    return jnp.transpose(stacked, (1, 0, 2)).reshape(stacked.shape[1], stacked.shape[0] * stacked.shape[2])


def _split(name, full):
    if name in ROW_SHARDED:
        return full.reshape(N_CHIP, full.shape[0] // N_CHIP, full.shape[1])
    return jnp.transpose(full.reshape(full.shape[0], N_CHIP, full.shape[1] // N_CHIP), (1, 0, 2))


GU = ("w_ffn1_in", "w_ffn2_in")
NARROW = ("w_in", "w_q_up")
TRANSPOSED = ("w_in",)
W_IN_SHARD = (Z_W - LANE + ROPE) // N_CHIP


MIX_SMALL = ("w_out", "w_pool_proj", "w_mla_proj", "w_q_up", "w_kv_up")
RIDES = {
    "ada_prologue": (("gather", ("w_ffn1_in",)),),
    "ffn1_up_a": (("gather", ("w_ffn1_out",)),),
    "ffn1_up_b": (("gather", ("w_in",)),),
    "mix_in": (("gather", MIX_SMALL),),
    "attn_fwd": (("gather", ("w_ffn2_in", "w_ffn2_out")),),
    "d_ffn2_h": (("swap", ("w_ffn2_out", "w_ffn2_in")),),
    "attn_bwd": (("exchange", ("w_ffn2_out", "w_ffn2_in")),),
    "d_mix_in": (("swap", MIX_SMALL),),
    "d_mix_h": (("exchange", MIX_SMALL), ("swap", ("w_in",))),
    "ffn1_dact": (("exchange", ("w_in",)), ("swap", ("w_ffn1_out",))),
    "d_ffn1_in_a": (("exchange", ("w_ffn1_out",)),),
    "d_ffn1_in_b": (("swap", ("w_ffn1_in@0",)),),
    "d_ffn1_h": (("exchange", ("w_ffn1_in@0",)), ("swap", ("w_ffn1_in@1",))),
    "bwd_norm1": (("exchange", ("w_ffn1_in@1",)),),
    "small_allreduce": (("join", tuple(n for n in BIG if n != "w_ffn1_in") + ("w_ffn1_in@0", "w_ffn1_in@1")),),
}


def _kname(n):
    base, _, part = n.partition("@")
    return KERNEL_NAME[base] + ("@" + part if part else "")


def _both(comms):
    if len(comms) == 1:
        return comms[0]
    ins, outs, sems, aliases, spans = [], [], [], {}, []
    for c in comms:
        spans.append((len(ins), len(c.ins), len(outs), len(c.out_shapes), len(sems), len(c.sems)))
        aliases.update({len(ins) + i: len(outs) + o for i, o in c.aliases.items()})
        ins, outs, sems = ins + c.ins, outs + c.out_shapes, sems + c.sems

    def each(which):
        def run(i_, o_, s_):
            for c, (ia, ni, oa, no, sa, ns) in zip(comms, spans):
                getattr(c, which)(i_[ia:ia + ni], o_[oa:oa + no], s_[sa:sa + ns])
        return run

    return _Comm(ins, outs, sems, each("start"), each("finish"), aliases)


class _ExchangePlan:
    def __init__(self, shards, cidx):
        self.shards, self.cidx = shards, cidx
        self.W, self.G, self.parts, self.pre, self.reduced, self.joined = {}, {}, {}, {}, {}, {}

    def grad(self, key, g):
        self.G[key] = g

    def rider(self, name):
        comms = []
        for kind, names in RIDES.get(name, ()):
            if kind == "gather":
                comms.append(_gather_comm([self.shards[n] for n in names]))
            elif kind == "swap":
                for n in names:
                    self.parts[n] = self._stacked(n)
                comms.append(_swap_comm([self.parts[n] for n in names]))
            elif kind == "exchange":
                comms.append(_exchange_comm([self.pre[n] for n in names]))
            else:
                comms.append(_join_comm([self.reduced[n] for n in names]))
        return _both(comms) if comms else None

    def landed(self, name, outs):
        at = 0
        for kind, names in RIDES[name]:
            for n, o in zip(names, outs[at:at + len(names)]):
                if kind == "gather":
                    self.W[KERNEL_NAME[n]] = self._to_kernel(n, o)
                elif kind == "swap":
                    self.pre[n] = _add_half(self.parts[n], o, self.cidx, name="rs_add_" + _kname(n))
                elif kind == "exchange":
                    self.reduced[n] = _sum_chips(o, self.cidx, name="rs_sum_" + _kname(n))
                else:
                    self.joined[n] = o
            at += len(names)

    @staticmethod
    def _to_kernel(n, stacked):
        if n in GU:
            return stacked
        if n in TRANSPOSED:
            return _w_in_stacked_to_kernel(stacked, W_IN_SHARD)
        full = _assemble(n, stacked)
        return {"w_q_up": _q_up_to_kernel, "w_kv_up": _kv_up_to_kernel}.get(n, lambda w: w)(full)

    def _stacked(self, n):
        g = self.G[_kname(n)]
        if n.partition("@")[0] in GU:
            return g
        if n in TRANSPOSED:
            return _w_in_kernel_to_stacked(g, W_IN_SHARD)
        full = {"w_q_up": _q_up_from_kernel, "w_kv_up": _kv_up_from_kernel}.get(n, lambda w: w)(g)
        return _split(n, full)


def kernel(x, c, positions, w_ada, b_ada, norm_ffn1, w_ffn1_in, w_ffn1_out, norm_mix, w_in, pool_grp, pool_scale, w_pool_proj, q_a_norm, w_q_up, kv_a_norm, w_kv_up, q_norm_nope, q_norm_rope, k_norm_nope, k_norm_rope, w_mla_proj, w_out, norm_ffn2, w_ffn2_in, w_ffn2_out, loss_target, m_w_ada, m_b_ada, m_norm_ffn1, m_w_ffn1_in, m_w_ffn1_out, m_norm_mix, m_w_in, m_pool_grp, m_pool_scale, m_w_pool_proj, m_q_a_norm, m_w_q_up, m_kv_a_norm, m_w_kv_up, m_q_norm_nope, m_q_norm_rope, m_k_norm_nope, m_k_norm_rope, m_w_mla_proj, m_w_out, m_norm_ffn2, m_w_ffn2_in, m_w_ffn2_out, v_w_ada, v_b_ada, v_norm_ffn1, v_w_ffn1_in, v_w_ffn1_out, v_norm_mix, v_w_in, v_pool_grp, v_pool_scale, v_w_pool_proj, v_q_a_norm, v_w_q_up, v_kv_a_norm, v_w_kv_up, v_q_norm_nope, v_q_norm_rope, v_k_norm_nope, v_k_norm_rope, v_w_mla_proj, v_w_out, v_norm_ffn2, v_w_ffn2_in, v_w_ffn2_out):
    args = dict(locals())
    wts = {n: args[n][0] for n in WEIGHTS}
    mom = {n: args["m_" + n][0] for n in WEIGHTS}
    var = {n: args["v_" + n][0] for n in WEIGHTS}
    nb, seq, dm = x.shape
    tokens = nb * seq
    xi, yi, ci = _coords()
    chip = 2 * xi + yi

    cidx = ci.astype(jnp.int32).reshape(1)
    plan = _ExchangePlan({n: _pad_rows(wts[n].T.astype(MX)) if n in TRANSPOSED else wts[n].astype(MX) for n in BIG}, cidx)
    plan.W["pool_grp"] = wts["pool_grp"].astype(MX)

    ncol = w_ada.shape[2]
    b_cols = lax.dynamic_slice_in_dim(wts["b_ada"].reshape(1, -1), chip * ncol, ncol, axis=1)
    c_slots, mod_slots = _run(plan, _ada_prologue, c, wts["w_ada"], b_cols, name="ada_prologue")
    c_all = c_slots[:, :nb].reshape(N_DEV * nb, dm)
    mod3 = jnp.transpose(mod_slots[:, :nb], (1, 0, 2)).reshape(nb, 9, dm)
    P = {"norm_ffn1": wts["norm_ffn1"].reshape(1, dm), "norm_mix": wts["norm_mix"].reshape(1, dm),
         "norm_ffn2": wts["norm_ffn2"].reshape(1, dm), "pool_scale": wts["pool_scale"].reshape(1, POOL_W),
         "q_a_norm": wts["q_a_norm"].reshape(1, QL), "kv_a_norm": wts["kv_a_norm"].reshape(1, KVL),
         "q_gain": _gain_slab(wts["q_norm_nope"].reshape(1, NOPE), wts["q_norm_rope"].reshape(1, ROPE)),
         "k_gain": _gain_slab(wts["k_norm_nope"].reshape(1, NOPE), wts["k_norm_rope"].reshape(1, ROPE))}
    cos, sin = _rope_tables(positions.reshape(tokens))

    loss8, grad_x, stats, d_pool_grp = _layer_fwd_bwd(x.reshape(tokens, dm), loss_target.reshape(tokens, dm), mod3, cos, sin, plan, P)

    slab = _pack_stats(*stats, loss8, name="pack_stats")
    red, dmod_rows, pool_total = _run(plan, _small_allreduce, slab, d_pool_grp.reshape(POOL_G * LANE, LANE), name="small_allreduce")
    grads_t = {n: plan.joined[n][:W_IN_SHARD] for n in TRANSPOSED}
    grads = {n: grads_t[n].T if n in TRANSPOSED else plan.joined[n] for n in BIG if n != "w_ffn1_in"}
    grads["w_ffn1_in"] = jnp.concatenate([plan.joined["w_ffn1_in@0"], plan.joined["w_ffn1_in@1"]], axis=0)
    dm_rows = dmod_rows.shape[0] // N_DEV
    dmod_all = dmod_rows.reshape(N_DEV, dm_rows, SLAB_W)[:, :9 * nb].reshape(N_DEV * nb, 9 * dm)
    dmod_cols = lax.dynamic_slice_in_dim(dmod_all, chip * ncol, ncol, axis=1)
    grads["w_ada"] = _ada_bwd(c_all, dmod_cols, name="ada_bwd")

    delta, new_m, new_v = {}, {}, {}
    as2d = lambda a: a.reshape(POOL_G * LANE, LANE) if a.ndim == 4 else a.reshape(1, -1)
    upd, loss_row = _small_update(red, dmod_rows, pool_total, nb,
                                  {n: tuple(as2d(args[p + n]) for p in ("", "m_", "v_")) for n in SMALL}, name="small_update")
    loss = loss_row[0, 0]
    for n in SMALL:
        grads[n], delta[n], new_m[n], new_v[n] = upd[n]
    for n in ("w_ada",) + BIG:
        if n in NARROW:
            res = _adamw(wts[n].T, grads_t[n] if n in TRANSPOSED else grads[n].T, mom[n].T, var[n].T, name="adamw_" + n)
            delta[n], new_m[n], new_v[n] = (r.T for r in res)
        else:
            delta[n], new_m[n], new_v[n] = _adamw(wts[n], grads[n], mom[n], var[n], name="adamw_" + n)

    def lead(a, n):
        return a.reshape((1,) + wts[n].shape)

    return (loss, grad_x.reshape(nb, seq, dm), *[lead(grads[n], n) for n in WEIGHTS], *[lead(delta[n], n) for n in WEIGHTS],
            *[lead(new_m[n], n) for n in WEIGHTS], *[lead(new_v[n], n) for n in WEIGHTS])
```

```python
import functools
import math

import jax
import jax.numpy as jnp
from jax import lax
from jax.experimental import pallas as pl
from jax.experimental.pallas import tpu as pltpu

F32 = jnp.float32
MX = jnp.bfloat16

D = 1024
DFF = 2816
NH = 8
POOL_W = 512
POOL_G = 4
QL = 384
KVL = 256
ROPE = 32
NOPE = 64
LANE = 128
SUBLANE = 8
EPS = 1e-6
ATTN_SCALE = 1.0 / math.sqrt(96.0)
NEG = -1e30

Z_UP, Z_QL, Z_KV, Z_KR, Z_GP, Z_GM, Z_W = 0, 512, 896, 1152, 1280, 2304, 3328
KR_LANE = 64
LAT_W = 1280

ADAM_LR, ADAM_B1, ADAM_B2, ADAM_EPS, ADAM_WD, ADAM_STEP = 0.001, 0.9, 0.999, 1e-08, 0.01, 10

VMEM_LIMIT = 48 * 1024 * 1024
MESH = pl.DeviceIdType.MESH
N_DEV = 8
N_CHIP = 4


class _Comm:
    def __init__(self, ins, out_shapes, sems, start, finish, aliases=None):
        self.ins, self.out_shapes, self.sems = list(ins), list(out_shapes), list(sems)
        self.start, self.finish = start, finish
        self.aliases = aliases or {}


def _pcall(body, *, name, out_shape, grid=(), in_specs=None, out_specs=None, scratch=(), grid_spec=None, aliases=None,
           comm=None):
    params = pltpu.CompilerParams(vmem_limit_bytes=VMEM_LIMIT)
    kw = dict(name=name, compiler_params=params)
    if comm is None:
        if aliases:
            kw["input_output_aliases"] = aliases
        if grid_spec is not None:
            return pl.pallas_call(body, grid_spec=grid_spec, out_shape=out_shape, **kw)
        return pl.pallas_call(body, grid=grid, in_specs=in_specs, out_specs=out_specs, scratch_shapes=scratch,
                              out_shape=out_shape, **kw)
    single = not isinstance(out_shape, (list, tuple))
    outs = [out_shape] if single else list(out_shape)
    ospecs = [out_specs] if single else list(out_specs)
    n_in, n_out, n_ci, n_co, n_scr = len(in_specs), len(outs), len(comm.ins), len(comm.out_shapes), len(scratch)
    io = dict(aliases or {})
    io.update({n_in + i: n_out + o for i, o in comm.aliases.items()})

    def riding(*refs):
        ins, cins = refs[:n_in], refs[n_in:n_in + n_ci]
        at = n_in + n_ci
        os_, couts = refs[at:at + n_out], refs[at + n_out:at + n_out + n_co]
        at += n_out + n_co
        scr, csems = refs[at:at + n_scr], refs[at + n_scr:]
        if grid:
            first = functools.reduce(jnp.logical_and, [pl.program_id(d) == 0 for d in range(len(grid))])
            last = functools.reduce(jnp.logical_and, [pl.program_id(d) == grid[d] - 1 for d in range(len(grid))])
            pl.when(first)(lambda: comm.start(cins, couts, csems))
            body(*ins, *os_, *scr)
            pl.when(last)(lambda: comm.finish(cins, couts, csems))
        else:
            comm.start(cins, couts, csems)
            body(*ins, *os_, *scr)
            comm.finish(cins, couts, csems)

    call = pl.pallas_call(riding, grid=grid, in_specs=list(in_specs) + [HBM_SPEC] * n_ci, out_specs=ospecs + [HBM_SPEC] * n_co,
                          out_shape=outs + comm.out_shapes, scratch_shapes=list(scratch) + comm.sems,
                          input_output_aliases=io, **kw)

    def run(*args):
        res = call(*args, *comm.ins)
        main = list(res[:n_out])
        return (main[0] if single else main), list(res[n_out:])

    return run


def _pick(dim, target):
    best = None
    for t in range(LANE, min(dim, target) + 1, LANE):
        if dim % t == 0:
            best = t
    return dim if best is None else best


def _sds(shape, dtype):
    return jax.ShapeDtypeStruct(shape, dtype)


def _dw(a, g, *, name, tm=512, tn=1024, out_t=False, comm=None):
    return _mm(a, g, mode="tn", out_dtype=F32, name=name, tm=tm, tn=tn, tk=a.shape[0], n_outer=True, out_t=out_t, comm=comm)


def _mm(a, b, *, mode, out_dtype, name, tm=1024, tn=1024, tk=4096, n_outer=False, out_t=False, comm=None):
    if mode == "nn":
        (M, K), (K2, N) = a.shape, b.shape
    elif mode == "nt":
        (M, K), (N, K2) = a.shape, b.shape
    else:
        (K, M), (K2, N) = a.shape, b.shape
    assert K == K2, (name, a.shape, b.shape)
    tm, tn, tk = _pick(M, tm), _pick(N, tn), _pick(K, tk)
    nk = K // tk
    if n_outer:
        ij = lambda g0, g1: (g1, g0)
        grid = (N // tn, M // tm, nk)
    else:
        ij = lambda g0, g1: (g0, g1)
        grid = (M // tm, N // tn, nk)
    if mode == "tn":
        a_spec = pl.BlockSpec((tk, tm), lambda g0, g1, k: (k, ij(g0, g1)[0]))
    else:
        a_spec = pl.BlockSpec((tm, tk), lambda g0, g1, k: (ij(g0, g1)[0], k))
    if mode == "nt":
        b_spec = pl.BlockSpec((tn, tk), lambda g0, g1, k: (ij(g0, g1)[1], k))
    else:
        b_spec = pl.BlockSpec((tk, tn), lambda g0, g1, k: (k, ij(g0, g1)[1]))
    if out_t:
        assert nk == 1, name
        o_spec = pl.BlockSpec((tn, tm), lambda g0, g1, k: ij(g0, g1)[::-1])
    else:
        o_spec = pl.BlockSpec((tm, tn), lambda g0, g1, k: ij(g0, g1))
    dn = {"nn": (((1,), (0,)), ((), ())), "nt": (((1,), (1,)), ((), ())), "tn": (((0,), (0,)), ((), ()))}[mode]

    def dot(a_ref, b_ref):
        return lax.dot_general(a_ref[...].astype(MX), b_ref[...].astype(MX), dn, preferred_element_type=F32)

    def body_one(a_ref, b_ref, o_ref):
        prod = dot(a_ref, b_ref)
        o_ref[...] = (prod.T if out_t else prod).astype(o_ref.dtype)

    def body_acc(a_ref, b_ref, o_ref, acc_ref):
        k = pl.program_id(2)
        part = dot(a_ref, b_ref)

        @pl.when(k == 0)
        def _():
            acc_ref[...] = part

        @pl.when(k > 0)
        def _():
            acc_ref[...] += part

        @pl.when(k == nk - 1)
        def _():
            o_ref[...] = acc_ref[...].astype(o_ref.dtype)

    return _pcall(body_one if nk == 1 else body_acc, name=name, out_shape=_sds((N, M) if out_t else (M, N), out_dtype), grid=grid,
                  in_specs=[a_spec, b_spec], out_specs=o_spec, scratch=[] if nk == 1 else [pltpu.VMEM((tm, tn), F32)],
                  comm=comm)(a, b)


def _gu_shard(q):
    return (q % 2) * 2 + q // 2


def _ffn_up(h, w_st, *, name, tm=512, comm=None):
    T, dm = h.shape
    hw = w_st.shape[2]
    tm = _pick(T, tm)

    def body(h_ref, wg_ref, wu_ref, gu_ref, a_ref):
        hv = h_ref[...]
        g = jnp.dot(hv, wg_ref[0], preferred_element_type=F32)
        u = jnp.dot(hv, wu_ref[0], preferred_element_type=F32)
        gu_ref[:, :hw] = g.astype(gu_ref.dtype)
        gu_ref[:, hw:] = u.astype(gu_ref.dtype)
        a_ref[...] = (g * _sigmoid(g) * u).astype(a_ref.dtype)

    return _pcall(body, name=name, grid=(T // tm, 2),
                  in_specs=[pl.BlockSpec((tm, dm), lambda i, j: (i, 0)), pl.BlockSpec((1, dm, hw), lambda i, j: (j, 0, 0)),
                            pl.BlockSpec((1, dm, hw), lambda i, j: (2 + j, 0, 0))],
                  out_specs=[pl.BlockSpec((tm, 2 * hw), lambda i, j: (i, j)), pl.BlockSpec((tm, hw), lambda i, j: (i, j))],
                  out_shape=[_sds((T, 4 * hw), MX), _sds((T, 2 * hw), MX)], comm=comm)(h, w_st, w_st)


def _ffn_up_first(x, mod3, gain, w_st, *, sub, name, tm=512, comm=None):
    T, dm = x.shape
    hw = w_st.shape[2]
    tm = _pick(T, tm)
    tps = (T // mod3.shape[0]) // tm

    def body(x_ref, mod_ref, n_ref, wg_ref, wu_ref, h_ref, gu_ref, a_ref):
        xv = x_ref[...]
        xn = xv * _rsq(xv) * n_ref[...]
        hv = (xn * (1.0 + mod_ref[0, 3 * sub + 1:3 * sub + 2, :]) + mod_ref[0, 3 * sub:3 * sub + 1, :]).astype(h_ref.dtype)
        h_ref[...] = hv
        g = jnp.dot(hv, wg_ref[0], preferred_element_type=F32)
        u = jnp.dot(hv, wu_ref[0], preferred_element_type=F32)
        gu_ref[:, :hw] = g.astype(gu_ref.dtype)
        gu_ref[:, hw:] = u.astype(gu_ref.dtype)
        a_ref[...] = (g * _sigmoid(g) * u).astype(a_ref.dtype)

    return _pcall(body, name=name, grid=(T // tm,),
                  in_specs=[_row_spec(tm, dm), pl.BlockSpec((1, 9, dm), lambda i: (i // tps, 0, 0)),
                            pl.BlockSpec((1, dm), lambda i: (0, 0)), pl.BlockSpec((1, dm, hw), lambda i: (0, 0, 0)),
                            pl.BlockSpec((1, dm, hw), lambda i: (2, 0, 0))],
                  out_specs=[_row_spec(tm, dm), pl.BlockSpec((tm, 2 * hw), lambda i: (i, 0)), pl.BlockSpec((tm, hw), lambda i: (i, 0))],
                  out_shape=[_sds((T, dm), MX), _sds((T, 4 * hw), MX), _sds((T, 2 * hw), MX)],
                  comm=comm)(x, mod3, gain, w_st, w_st)


def _ffn_up_second(h, w_st, gu, a, *, name, tm=512, comm=None):
    T, dm = h.shape
    hw = w_st.shape[2]
    tm = _pick(T, tm)

    def body(h_ref, wg_ref, wu_ref, gu_in, a_in, gu_ref, a_ref):
        hv = h_ref[...]
        g = jnp.dot(hv, wg_ref[0], preferred_element_type=F32)
        u = jnp.dot(hv, wu_ref[0], preferred_element_type=F32)
        gu_ref[:, :hw] = g.astype(gu_ref.dtype)
        gu_ref[:, hw:] = u.astype(gu_ref.dtype)
        a_ref[...] = (g * _sigmoid(g) * u).astype(a_ref.dtype)

    return _pcall(body, name=name, grid=(T // tm,),
                  in_specs=[_row_spec(tm, dm), pl.BlockSpec((1, dm, hw), lambda i: (1, 0, 0)),
                            pl.BlockSpec((1, dm, hw), lambda i: (3, 0, 0)), HBM_SPEC, HBM_SPEC],
                  out_specs=[pl.BlockSpec((tm, 2 * hw), lambda i: (i, 1)), pl.BlockSpec((tm, hw), lambda i: (i, 1))],
                  out_shape=[_sds(gu.shape, gu.dtype), _sds(a.shape, a.dtype)], aliases={3: 0, 4: 1},
                  comm=comm)(h, w_st, w_st, gu, a)


def _ffn_dact(df, gu, w_out, *, name, tm=512, comm=None):
    T, dm = df.shape
    hw = gu.shape[1] // 4
    tm = _pick(T, tm)

    def body(df_ref, gu_ref, wo_ref, dgu_ref):
        da = lax.dot_general(df_ref[...], wo_ref[...], (((1,), (1,)), ((), ())), preferred_element_type=F32)
        g = gu_ref[:, :hw].astype(F32)
        u = gu_ref[:, hw:].astype(F32)
        s = _sigmoid(g)
        dgu_ref[:, :hw] = (da * u * (s * (1.0 + g * (1.0 - s)))).astype(dgu_ref.dtype)
        dgu_ref[:, hw:] = (da * (g * s)).astype(dgu_ref.dtype)

    return _pcall(body, name=name, grid=(T // tm, 2),
                  in_specs=[pl.BlockSpec((tm, dm), lambda i, j: (i, 0)), pl.BlockSpec((tm, 2 * hw), lambda i, j: (i, j)),
                            pl.BlockSpec((hw, dm), lambda i, j: (j, 0))],
                  out_specs=pl.BlockSpec((tm, 2 * hw), lambda i, j: (i, j)), out_shape=_sds(gu.shape, MX),
                  comm=comm)(df, gu, w_out)


def _ffn_dh(dgu, w_st, *, name, tm=1024, comm=None):
    T = dgu.shape[0]
    _, dm, hw = w_st.shape
    tm = _pick(T, tm)

    def body(d_ref, w_ref, o_ref, acc_ref):
        q = pl.program_id(1)
        part = lax.dot_general(d_ref[...], w_ref[0], (((1,), (1,)), ((), ())), preferred_element_type=F32)

        @pl.when(q == 0)
        def _():
            acc_ref[...] = part

        @pl.when(jnp.logical_and(q > 0, q < 3))
        def _():
            acc_ref[...] += part

        @pl.when(q == 3)
        def _():
            o_ref[...] = acc_ref[...] + part

    return _pcall(body, name=name, grid=(T // tm, 4),
                  in_specs=[pl.BlockSpec((tm, hw), lambda i, q: (i, q)), pl.BlockSpec((1, dm, hw), lambda i, q: (_gu_shard(q), 0, 0))],
                  out_specs=pl.BlockSpec((tm, dm), lambda i, q: (i, 0)), out_shape=_sds((T, dm), F32),
                  scratch=[pltpu.VMEM((tm, dm), F32)], comm=comm)(dgu, w_st)


def _ffn_dw_in(h, dgu, *, name, rows=None, tm=512, comm=None):
    T, dm = h.shape
    hw = dgu.shape[1] // 4
    first, count = rows if rows is not None else (0, dm)
    tm = _pick(count, tm)
    skip = first // tm

    def body(h_ref, d_ref, o_ref):
        o_ref[0] = lax.dot_general(h_ref[...], d_ref[...], (((0,), (0,)), ((), ())), preferred_element_type=F32)

    return _pcall(body, name=name, grid=(4, count // tm),
                  in_specs=[pl.BlockSpec((T, tm), lambda q, i: (0, skip + i)), pl.BlockSpec((T, hw), lambda q, i: (0, q))],
                  out_specs=pl.BlockSpec((1, tm, hw), lambda q, i: (_gu_shard(q), i, 0)),
                  out_shape=_sds((4, count, hw), F32), comm=comm)(h, dgu)


def _rsq(x):
    return lax.rsqrt(jnp.mean(x * x, axis=-1, keepdims=True) + EPS)


def _sigmoid(x):
    return 1.0 / (1.0 + jnp.exp(-x))


def _row_spec(tm, w):
    return pl.BlockSpec((tm, w), lambda i: (i, 0))


def _mm_resid_norm(a, w, x_prev, mod3, gain, *, sub, coef, name, tm=512, comm=None):
    T, k = a.shape
    dm = w.shape[1]
    tm = _pick(T, tm)
    tps = (T // mod3.shape[0]) // tm

    def body(a_ref, w_ref, x_ref, mod_ref, n_ref, f_ref, xo_ref, h_ref):
        f = jnp.dot(a_ref[...], w_ref[...], preferred_element_type=F32)
        f_ref[...] = f
        x = x_ref[...] + coef * mod_ref[0, 3 * sub - 1:3 * sub, :] * f
        xo_ref[...] = x
        xn = x * _rsq(x) * n_ref[...]
        h_ref[...] = (xn * (1.0 + mod_ref[0, 3 * sub + 1:3 * sub + 2, :]) + mod_ref[0, 3 * sub:3 * sub + 1, :]).astype(h_ref.dtype)

    row = _row_spec(tm, dm)
    return _pcall(body, name=name, grid=(T // tm,),
                  in_specs=[_row_spec(tm, k), pl.BlockSpec((k, dm), lambda i: (0, 0)), row,
                            pl.BlockSpec((1, 9, dm), lambda i: (i // tps, 0, 0)), pl.BlockSpec((1, dm), lambda i: (0, 0))],
                  out_specs=[row, row, row], out_shape=[_sds((T, dm), F32), _sds((T, dm), F32), _sds((T, dm), MX)],
                  comm=comm)(a, w, x_prev, mod3, gain)


def _mm_loss(a, w, x2, tgt, mod3, *, name, tm=512):
    T, k = a.shape
    dm = w.shape[1]
    tm = _pick(T, tm)
    tps = (T // mod3.shape[0]) // tm
    mod_spec = pl.BlockSpec((1, 9, dm), lambda i: (i // tps, 0, 0))
    row = _row_spec(tm, dm)
    stat_spec = pl.BlockSpec((1, SUBLANE, dm), lambda i: (i // tps, 0, 0))
    loss_spec = pl.BlockSpec((SUBLANE, LANE), lambda i: (0, 0))

    def body(a_ref, w_ref, x_ref, t_ref, mod_ref, dy_ref, df_ref, st_ref, loss_ref):
        i = pl.program_id(0)
        g = mod_ref[0, 8:9, :]
        f = jnp.dot(a_ref[...], w_ref[...], preferred_element_type=F32)
        err = x_ref[...] + 0.5 * g * f - t_ref[...]
        dy = err * (1.0 / dm)
        dy_ref[...] = dy
        df_ref[...] = (0.5 * g * dy).astype(df_ref.dtype)
        dgate = jnp.sum(0.5 * dy * f, axis=0, keepdims=True)
        part = 0.5 * jnp.sum(jnp.sum(err * err, axis=0, keepdims=True), axis=1, keepdims=True) * (1.0 / dm)

        @pl.when(i % tps == 0)
        def _():
            st_ref[...] = jnp.zeros_like(st_ref)

        @pl.when(i == 0)
        def _():
            loss_ref[...] = jnp.zeros_like(loss_ref)

        st_ref[0, 0:1, :] += dgate
        loss_ref[...] += jnp.broadcast_to(part, loss_ref.shape)

    return _pcall(body, name=name, grid=(T // tm,),
                  in_specs=[_row_spec(tm, k), pl.BlockSpec((k, dm), lambda i: (0, 0)), row, row, mod_spec],
                  out_specs=[row, row, stat_spec, loss_spec],
                  out_shape=[_sds((T, dm), F32), _sds((T, dm), MX), _sds((mod3.shape[0], SUBLANE, dm), F32),
                             _sds((SUBLANE, LANE), F32)])(a, w, x2, tgt, mod3)


def _norm_bwd_tail(dhv, x_ref, dxi_ref, f_ref, mod_ref, n_ref, dx_ref, df_ref, st_ref, *, first, sub, coef):
    x = x_ref[...]
    r = _rsq(x)
    xhat = x * r
    n = n_ref[...]
    d_shift = jnp.sum(dhv, axis=0, keepdims=True)
    d_scale = jnp.sum(dhv * (xhat * n), axis=0, keepdims=True)
    dxn = dhv * (1.0 + mod_ref[0, 3 * sub + 1:3 * sub + 2, :])
    d_gain = jnp.sum(dxn * xhat, axis=0, keepdims=True)
    dxhat = dxn * n
    dx = dxi_ref[...] + r * (dxhat - xhat * jnp.mean(dxhat * xhat, axis=-1, keepdims=True))
    dx_ref[...] = dx

    @pl.when(first)
    def _():
        st_ref[...] = jnp.zeros_like(st_ref)

    st_ref[0, 0:1, :] += d_shift
    st_ref[0, 1:2, :] += d_scale
    st_ref[0, 2:3, :] += d_gain
    if f_ref is not None:
        st_ref[0, 3:4, :] += jnp.sum(coef * dx * f_ref[...], axis=0, keepdims=True)
        df_ref[...] = (coef * mod_ref[0, 3 * sub - 1:3 * sub, :] * dx).astype(df_ref.dtype)


def _ffn_dh_norm(dgu, w_st, x_cur, dx_in, f_prev, mod3, gain, *, sub, coef, name, tm=512, comm=None):
    T = dgu.shape[0]
    _, dm, hw = w_st.shape
    tm = _pick(T, tm)
    nb = mod3.shape[0]
    tps = (T // nb) // tm

    def body(d_ref, w_ref, x_ref, dxi_ref, f_ref, mod_ref, n_ref, dx_ref, df_ref, st_ref, acc_ref):
        i = pl.program_id(0)
        q = pl.program_id(1)
        part = lax.dot_general(d_ref[...], w_ref[0], (((1,), (1,)), ((), ())), preferred_element_type=F32)

        @pl.when(q == 0)
        def _():
            acc_ref[...] = part

        @pl.when(q > 0)
        def _():
            acc_ref[...] += part

        @pl.when(q == 3)
        def _():
            _norm_bwd_tail(acc_ref[...], x_ref, dxi_ref, f_ref, mod_ref, n_ref, dx_ref, df_ref, st_ref,
                           first=i % tps == 0, sub=sub, coef=coef)

    row = pl.BlockSpec((tm, dm), lambda i, q: (i, 0))
    return _pcall(body, name=name, grid=(T // tm, 4),
                  in_specs=[pl.BlockSpec((tm, hw), lambda i, q: (i, q)), pl.BlockSpec((1, dm, hw), lambda i, q: (_gu_shard(q), 0, 0)),
                            row, row, row, pl.BlockSpec((1, 9, dm), lambda i, q: (i // tps, 0, 0)),
                            pl.BlockSpec((1, dm), lambda i, q: (0, 0))],
                  out_specs=[row, row, pl.BlockSpec((1, SUBLANE, dm), lambda i, q: (i // tps, 0, 0))],
                  out_shape=[_sds((T, dm), F32), _sds((T, dm), MX), _sds((nb, SUBLANE, dm), F32)],
                  scratch=[pltpu.VMEM((tm, dm), F32)], comm=comm)(dgu, w_st, x_cur, dx_in, f_prev, mod3, gain)


def _mm_norm_bwd(a, b, x_cur, dx_in, f_prev, mod3, gain, *, sub, coef, name, tm=512, comm=None):
    T, k = a.shape
    dm = b.shape[1]
    tm = _pick(T, tm)
    nb = mod3.shape[0]
    tps = (T // nb) // tm

    def body(a_ref, b_ref, x_ref, dxi_ref, f_ref, mod_ref, n_ref, dx_ref, df_ref, st_ref):
        dh = jnp.dot(a_ref[...], b_ref[...], preferred_element_type=F32)
        _norm_bwd_tail(dh, x_ref, dxi_ref, f_ref, mod_ref, n_ref, dx_ref, df_ref, st_ref,
                       first=pl.program_id(0) % tps == 0, sub=sub, coef=coef)

    row = _row_spec(tm, dm)
    return _pcall(body, name=name, grid=(T // tm,),
                  in_specs=[_row_spec(tm, k), pl.BlockSpec((k, dm), lambda i: (0, 0)), row, row, row,
                            pl.BlockSpec((1, 9, dm), lambda i: (i // tps, 0, 0)), pl.BlockSpec((1, dm), lambda i: (0, 0))],
                  out_specs=[row, row, pl.BlockSpec((1, SUBLANE, dm), lambda i: (i // tps, 0, 0))],
                  out_shape=[_sds((T, dm), F32), _sds((T, dm), MX), _sds((nb, SUBLANE, dm), F32)],
                  comm=comm)(a, b, x_cur, dx_in, f_prev, mod3, gain)


def _bwd_block(x_cur, dh, dx_in, f_prev, mod3, gain, *, sub, coef, name, tm=256, comm=None):
    T, dm = x_cur.shape
    nb = mod3.shape[0]
    tps = (T // nb) // tm
    has_f = f_prev is not None
    mod_spec = pl.BlockSpec((1, 9, dm), lambda i: (i // tps, 0, 0))
    vec_spec = pl.BlockSpec((1, dm), lambda i: (0, 0))
    stat_spec = pl.BlockSpec((1, SUBLANE, dm), lambda i: (i // tps, 0, 0))
    row = _row_spec(tm, dm)

    def body(*refs):
        if has_f:
            x_ref, dh_ref, dxi_ref, f_ref, mod_ref, n_ref, dx_ref, df_ref, st_ref = refs
        else:
            x_ref, dh_ref, dxi_ref, mod_ref, n_ref, dx_ref, st_ref = refs
            f_ref = df_ref = None
        _norm_bwd_tail(dh_ref[...], x_ref, dxi_ref, f_ref, mod_ref, n_ref, dx_ref, df_ref, st_ref,
                       first=pl.program_id(0) % tps == 0, sub=sub, coef=coef)

    st_shape = _sds((nb, SUBLANE, dm), F32)
    if has_f:
        return _pcall(body, name=name, grid=(T // tm,), in_specs=[row, row, row, row, mod_spec, vec_spec],
                      out_specs=[row, row, stat_spec],
                      out_shape=[_sds((T, dm), F32), _sds((T, dm), MX), st_shape], comm=comm)(x_cur, dh, dx_in, f_prev, mod3, gain)
    return _pcall(body, name=name, grid=(T // tm,), in_specs=[row, row, row, mod_spec, vec_spec],
                  out_specs=[row, stat_spec], out_shape=[_sds((T, dm), F32), st_shape], comm=comm)(x_cur, dh, dx_in, mod3, gain)


def _mix_out_fwd(z, br_pool, attn, w_mla, w_out, x_prev, mod3, gain, *, sub, name, tm=512):
    T = z.shape[0]
    dm = w_out.shape[1]
    tm = _pick(T, tm)
    tps = (T // mod3.shape[0]) // tm
    whole = lambda a: pl.BlockSpec(a.shape, lambda i: (0, 0))

    def body(z_ref, bp_ref, at_ref, wm_ref, wo_ref, x_ref, mod_ref, n_ref, bm_ref, mg_ref, mo_ref, xo_ref, h_ref):
        bm = jnp.dot(at_ref[...], wm_ref[...], preferred_element_type=F32).astype(MX)
        bm_ref[...] = bm
        gp = z_ref[:, Z_GP:Z_GP + D].astype(F32)
        gm = z_ref[:, Z_GM:Z_GM + D].astype(F32)
        merged = (_sigmoid(gp) * bp_ref[...] + _sigmoid(gm) * bm).astype(MX)
        mg_ref[...] = merged
        mo = jnp.dot(merged, wo_ref[...], preferred_element_type=F32)
        mo_ref[...] = mo
        x = x_ref[...] + mod_ref[0, 3 * sub - 1:3 * sub, :] * mo
        xo_ref[...] = x
        xn = x * _rsq(x) * n_ref[...]
        h_ref[...] = (xn * (1.0 + mod_ref[0, 3 * sub + 1:3 * sub + 2, :]) + mod_ref[0, 3 * sub:3 * sub + 1, :]).astype(h_ref.dtype)

    row = _row_spec(tm, dm)
    return _pcall(body, name=name, grid=(T // tm,),
                  in_specs=[_row_spec(tm, Z_W), row, _row_spec(tm, attn.shape[1]), whole(w_mla), whole(w_out), row,
                            pl.BlockSpec((1, 9, dm), lambda i: (i // tps, 0, 0)), pl.BlockSpec((1, dm), lambda i: (0, 0))],
                  out_specs=[row, row, row, row, row],
                  out_shape=[_sds((T, dm), MX), _sds((T, dm), MX), _sds((T, dm), F32), _sds((T, dm), F32), _sds((T, dm), MX)],
                  )(z, br_pool, attn, w_mla, w_out, x_prev, mod3, gain)


def _mix_out_bwd(dmo, w_out, z, br_pool, br_mla, w_mla, *, name, tm=512):
    T = z.shape[0]
    tm = _pick(T, tm)
    whole = lambda a: pl.BlockSpec(a.shape, lambda i: (0, 0))
    nt = (((1,), (1,)), ((), ()))

    def body(dmo_ref, wo_ref, z_ref, bp_ref, bm_ref, wm_ref, dbp_ref, dbm_ref, dg_ref, dat_ref):
        dm = lax.dot_general(dmo_ref[...], wo_ref[...], nt, preferred_element_type=F32)
        sp = _sigmoid(z_ref[:, Z_GP:Z_GP + D].astype(F32))
        sm = _sigmoid(z_ref[:, Z_GM:Z_GM + D].astype(F32))
        dbp_ref[...] = (dm * sp).astype(dbp_ref.dtype)
        dbm = (dm * sm).astype(MX)
        dbm_ref[...] = dbm
        dg_ref[:, :D] = (dm * bp_ref[...].astype(F32) * sp * (1.0 - sp)).astype(dg_ref.dtype)
        dg_ref[:, D:] = (dm * bm_ref[...].astype(F32) * sm * (1.0 - sm)).astype(dg_ref.dtype)
        dat_ref[...] = lax.dot_general(dbm, wm_ref[...], nt, preferred_element_type=F32)

    row = _row_spec(tm, D)
    return _pcall(body, name=name, grid=(T // tm,),
                  in_specs=[row, whole(w_out), _row_spec(tm, Z_W), row, row, whole(w_mla)],
                  out_specs=[row, row, _row_spec(tm, 2 * D), _row_spec(tm, w_mla.shape[0])],
                  out_shape=[_sds((T, D), MX), _sds((T, D), MX), _sds((T, 2 * D), MX), _sds((T, w_mla.shape[0]), F32)],
                  )(dmo, w_out, z, br_pool, br_mla, w_mla)


def _shift_down(x, k, row):
    return jnp.where(row >= k, pltpu.roll(x, k, 0), 0.0)


def _shift_up(x, k, row, n):
    return jnp.where(row < n - k, pltpu.roll(x, n - k, 0), 0.0)


def _pool_fwd(z, pool_grp, pool_scale, *, nb, name):
    T = z.shape[0]
    S = T // nb
    blk = pl.BlockSpec((S, LANE), lambda b, g: (b, g))

    def body(u_ref, w_ref, s_ref, pooled_ref, mixed_ref, scaled_ref):
        g = pl.program_id(1)
        u = u_ref[...].astype(F32)
        row = lax.broadcasted_iota(jnp.int32, u.shape, 0)
        s2 = u + _shift_down(u, 1, row)
        s4 = s2 + _shift_down(s2, 2, row)
        s8 = s4 + _shift_down(s4, 4, row)
        s16 = s8 + _shift_down(s8, 8, row)
        win = jnp.where(g == 0, s2, jnp.where(g == 1, s4, jnp.where(g == 2, s8, s16)))
        width = lax.shift_left(jnp.int32(2), g)
        cnt = jnp.minimum(row + 1, width).astype(F32)
        pooled = (win / cnt - u).astype(MX)
        pooled_ref[...] = pooled
        mixed = jnp.dot(pooled, w_ref[0], preferred_element_type=F32)
        mixed_ref[...] = mixed
        scaled_ref[...] = (mixed * s_ref[...]).astype(scaled_ref.dtype)

    return _pcall(body, name=name, grid=(nb, POOL_G),
                  in_specs=[blk, pl.BlockSpec((1, LANE, LANE), lambda b, g: (g, 0, 0)),
                            pl.BlockSpec((1, LANE), lambda b, g: (0, g))],
                  out_specs=[blk, blk, blk],
                  out_shape=[_sds((T, POOL_W), MX), _sds((T, POOL_W), F32), _sds((T, POOL_W), MX)])(z, pool_grp, pool_scale)


def _pool_bwd(dscaled, mixed, pooled, pool_grp, pool_scale, *, nb, name):
    T = dscaled.shape[0]
    S = T // nb
    blk = pl.BlockSpec((S, LANE), lambda g, b: (b, g))

    def body(ds_ref, mixed_ref, pooled_ref, w_ref, s_ref, du_ref, dw_ref, dsc_ref):
        g = pl.program_id(0)
        b = pl.program_id(1)
        ds = ds_ref[...]
        dsc_ref[0] = jnp.sum(ds * mixed_ref[...], axis=0, keepdims=True)
        dmixed = (ds * s_ref[...]).astype(MX)
        dw = lax.dot_general(pooled_ref[...], dmixed, (((0,), (0,)), ((), ())), preferred_element_type=F32)

        @pl.when(b == 0)
        def _():
            dw_ref[0] = dw

        @pl.when(b > 0)
        def _():
            dw_ref[0] += dw
        dpooled = lax.dot_general(dmixed, w_ref[0], (((1,), (1,)), ((), ())), preferred_element_type=F32)
        row = lax.broadcasted_iota(jnp.int32, dpooled.shape, 0)
        width = lax.shift_left(jnp.int32(2), g)
        q = dpooled / jnp.minimum(row + 1, width).astype(F32)
        r2 = q + _shift_up(q, 1, row, S)
        r4 = r2 + _shift_up(r2, 2, row, S)
        r8 = r4 + _shift_up(r4, 4, row, S)
        r16 = r8 + _shift_up(r8, 8, row, S)
        win = jnp.where(g == 0, r2, jnp.where(g == 1, r4, jnp.where(g == 2, r8, r16)))
        du_ref[...] = (win - dpooled).astype(du_ref.dtype)

    return _pcall(body, name=name, grid=(POOL_G, nb),
                  in_specs=[blk, blk, blk, pl.BlockSpec((1, LANE, LANE), lambda g, b: (g, 0, 0)),
                            pl.BlockSpec((1, LANE), lambda g, b: (0, g))],
                  out_specs=[blk, pl.BlockSpec((1, LANE, LANE), lambda g, b: (g, 0, 0)),
                             pl.BlockSpec((1, 1, LANE), lambda g, b: (b, 0, g))],
                  out_shape=[_sds((T, POOL_W), MX), _sds((POOL_G, LANE, LANE), F32), _sds((nb, 1, POOL_W), F32)],
                  )(dscaled, mixed, pooled, pool_grp, pool_scale)


def _lane_masks(shape):
    lane = lax.broadcasted_iota(jnp.int32, shape, len(shape) - 1)
    m_n = lane < NOPE
    m_r = jnp.logical_and(lane >= KR_LANE, lane < KR_LANE + ROPE)
    first_half = lane < KR_LANE + ROPE // 2
    return m_n, m_r, first_half


def _rot(y, first_half):
    return jnp.where(first_half, -pltpu.roll(y, LANE - ROPE // 2, 1), pltpu.roll(y, ROPE // 2, 1))


def _rot_t(v, first_half, m_r):
    return jnp.where(m_r, jnp.where(first_half, pltpu.roll(v, LANE - ROPE // 2, 1), -pltpu.roll(v, ROPE // 2, 1)), 0.0)


def _mla_in_fwd(z, w_q, w_kv, qa_gain, kva_gain, cos, sin, q_gain, k_gain, *, name, tm=256):
    T = z.shape[0]
    slab = pl.BlockSpec((tm, LANE), lambda i: (i, 0))
    vec = pl.BlockSpec((1, LANE), lambda i: (0, 0))
    whole = lambda a: pl.BlockSpec(a.shape, lambda i: (0, 0))

    def body(z_ref, wq_ref, wkv_ref, qa_ref, kva_ref, cos_ref, sin_ref, qg_ref, kg_ref,
             qn_ref, kvn_ref, qp_ref, kvp_ref, q_ref, k_ref, v_ref):
        ql = z_ref[:, Z_QL:Z_QL + QL].astype(F32)
        kvl = z_ref[:, Z_KV:Z_KV + KVL].astype(F32)
        qn = (ql * _rsq(ql) * qa_ref[...]).astype(MX)
        kvn = (kvl * _rsq(kvl) * kva_ref[...]).astype(MX)
        qn_ref[...] = qn
        kvn_ref[...] = kvn
        qp = jnp.dot(qn, wq_ref[...], preferred_element_type=F32)
        kvp = jnp.dot(kvn, wkv_ref[...], preferred_element_type=F32)
        qp_ref[...] = qp
        kvp_ref[...] = kvp
        m_n, m_r, first_half = _lane_masks((tm, LANE))
        c = cos_ref[...]
        s = sin_ref[...]
        qg = qg_ref[...]
        kg = kg_ref[...]
        xr = z_ref[:, Z_KR:Z_KR + LANE].astype(F32)
        rr = lax.rsqrt(jnp.sum(xr * xr, axis=-1, keepdims=True) * (1.0 / ROPE) + EPS)
        yr = xr * rr * kg
        kr = jnp.where(m_r, yr * c + _rot(yr, first_half) * s, 0.0)
        for h in range(NH):
            x = qp[:, h * LANE:(h + 1) * LANE]
            x2 = x * x
            rn = lax.rsqrt(jnp.sum(jnp.where(m_n, x2, 0.0), axis=-1, keepdims=True) * (1.0 / NOPE) + EPS)
            rq = lax.rsqrt(jnp.sum(jnp.where(m_r, x2, 0.0), axis=-1, keepdims=True) * (1.0 / ROPE) + EPS)
            y = x * jnp.where(m_n, rn, jnp.where(m_r, rq, 0.0)) * qg
            q_ref[:, h * LANE:(h + 1) * LANE] = ((y * c + _rot(y, first_half) * s) * ATTN_SCALE).astype(q_ref.dtype)
            xk = kvp[:, h * LANE:(h + 1) * LANE]
            rk = lax.rsqrt(jnp.sum(jnp.where(m_n, xk * xk, 0.0), axis=-1, keepdims=True) * (1.0 / NOPE) + EPS)
            k_ref[:, h * LANE:(h + 1) * LANE] = (jnp.where(m_n, xk * rk * kg, 0.0) + kr).astype(k_ref.dtype)
        v_ref[...] = kvp[:, NH * LANE:].astype(v_ref.dtype)

    return _pcall(body, name=name, grid=(T // tm,),
                  in_specs=[_row_spec(tm, LAT_W), whole(w_q), whole(w_kv), whole(qa_gain), whole(kva_gain), slab, slab, vec, vec],
                  out_specs=[_row_spec(tm, QL), _row_spec(tm, KVL), _row_spec(tm, NH * LANE), _row_spec(tm, NH * LANE + NH * NOPE),
                             _row_spec(tm, NH * LANE), _row_spec(tm, NH * LANE), _row_spec(tm, NH * NOPE)],
                  out_shape=[_sds((T, QL), MX), _sds((T, KVL), MX), _sds((T, NH * LANE), F32), _sds((T, NH * LANE + NH * NOPE), F32),
                             _sds((T, NH * LANE), MX), _sds((T, NH * LANE), MX), _sds((T, NH * NOPE), MX)],
                  )(z, w_q, w_kv, qa_gain, kva_gain, cos, sin, q_gain, k_gain)


def _mla_in_bwd(dq, dk, dv, qp, kvp, z, w_q, w_kv, qa_gain, kva_gain, cos, sin, q_gain, k_gain, *, nb, name, tm=256):
    T = dq.shape[0]
    tps = (T // nb) // tm
    slab = pl.BlockSpec((tm, LANE), lambda i: (i, 0))
    vec = pl.BlockSpec((1, LANE), lambda i: (0, 0))
    whole = lambda a: pl.BlockSpec(a.shape, lambda i: (0, 0))

    def latent_bwd(x, dy, gain):
        r = _rsq(x)
        xhat = x * r
        dxhat = dy * gain
        return r * (dxhat - xhat * jnp.mean(dxhat * xhat, axis=-1, keepdims=True)), jnp.sum(dy * xhat, axis=0, keepdims=True)

    def body(dq_ref, dk_ref, dv_ref, qp_ref, kvp_ref, z_ref, wq_ref, wkv_ref, qa_ref, kva_ref, cos_ref, sin_ref, qg_ref, kg_ref,
             dqp_ref, dkvp_ref, dkr_ref, dql_ref, dkvl_ref, st_ref, sq_ref, sk_ref):
        i = pl.program_id(0)
        qp, kvp = qp_ref, kvp_ref
        m_n, m_r, first_half = _lane_masks((tm, LANE))
        c = cos_ref[...]
        s = sin_ref[...]
        qg = qg_ref[...]
        kg = kg_ref[...]
        dqg = jnp.zeros((1, LANE), F32)
        dkg = jnp.zeros((1, LANE), F32)
        dkr_sum = jnp.zeros((tm, LANE), F32)
        for h in range(NH):
            x = qp[:, h * LANE:(h + 1) * LANE]
            x2 = x * x
            rn = lax.rsqrt(jnp.sum(jnp.where(m_n, x2, 0.0), axis=-1, keepdims=True) * (1.0 / NOPE) + EPS)
            rq = lax.rsqrt(jnp.sum(jnp.where(m_r, x2, 0.0), axis=-1, keepdims=True) * (1.0 / ROPE) + EPS)
            rfac = jnp.where(m_n, rn, jnp.where(m_r, rq, 0.0))
            xhat = x * rfac
            do = dq_ref[:, h * LANE:(h + 1) * LANE] * ATTN_SCALE
            dy = do * c + _rot_t(do * s, first_half, m_r)
            dqg = dqg + jnp.sum(dy * xhat, axis=0, keepdims=True)
            dxhat = dy * qg
            t = dxhat * xhat
            mean_n = jnp.sum(jnp.where(m_n, t, 0.0), axis=-1, keepdims=True) * (1.0 / NOPE)
            mean_r = jnp.sum(jnp.where(m_r, t, 0.0), axis=-1, keepdims=True) * (1.0 / ROPE)
            dqp_ref[:, h * LANE:(h + 1) * LANE] = (
                rfac * (dxhat - xhat * jnp.where(m_n, mean_n, jnp.where(m_r, mean_r, 0.0)))).astype(dqp_ref.dtype)

            xk = kvp[:, h * LANE:(h + 1) * LANE]
            rk = lax.rsqrt(jnp.sum(jnp.where(m_n, xk * xk, 0.0), axis=-1, keepdims=True) * (1.0 / NOPE) + EPS)
            khat = jnp.where(m_n, xk * rk, 0.0)
            dko = dk_ref[:, h * LANE:(h + 1) * LANE]
            dkn = jnp.where(m_n, dko, 0.0)
            dkg = dkg + jnp.sum(dkn * khat, axis=0, keepdims=True)
            dkhat = dkn * kg
            mean_k = jnp.sum(dkhat * khat, axis=-1, keepdims=True) * (1.0 / NOPE)
            dkvp_ref[:, h * LANE:(h + 1) * LANE] = jnp.where(m_n, rk * (dkhat - khat * mean_k), 0.0).astype(dkvp_ref.dtype)
            dkr_sum = dkr_sum + jnp.where(m_r, dko, 0.0)
        dkvp_ref[:, NH * LANE:] = dv_ref[...].astype(dkvp_ref.dtype)

        xr = z_ref[:, Z_KR:Z_KR + LANE].astype(F32)
        rr = lax.rsqrt(jnp.sum(xr * xr, axis=-1, keepdims=True) * (1.0 / ROPE) + EPS)
        rhat = xr * rr
        dyr = dkr_sum * c + _rot_t(dkr_sum * s, first_half, m_r)
        dkg = dkg + jnp.sum(dyr * rhat, axis=0, keepdims=True)
        drhat = dyr * kg
        mean_kr = jnp.sum(drhat * rhat, axis=-1, keepdims=True) * (1.0 / ROPE)
        dkr_ref[...] = jnp.where(m_r, rr * (drhat - rhat * mean_kr), 0.0).astype(dkr_ref.dtype)

        nt = (((1,), (1,)), ((), ()))
        dqn = lax.dot_general(dqp_ref[...], wq_ref[...], nt, preferred_element_type=F32)
        dkvn = lax.dot_general(dkvp_ref[...], wkv_ref[...], nt, preferred_element_type=F32)
        dql, dqa = latent_bwd(z_ref[:, Z_QL:Z_QL + QL].astype(F32), dqn, qa_ref[...])
        dkvl, dkva = latent_bwd(z_ref[:, Z_KV:Z_KV + KVL].astype(F32), dkvn, kva_ref[...])
        dql_ref[...] = dql.astype(dql_ref.dtype)
        dkvl_ref[...] = dkvl.astype(dkvl_ref.dtype)

        @pl.when(i % tps == 0)
        def _():
            st_ref[...] = jnp.zeros_like(st_ref)
            sq_ref[...] = jnp.zeros_like(sq_ref)
            sk_ref[...] = jnp.zeros_like(sk_ref)

        st_ref[0, 0:1, :] += dqg
        st_ref[0, 1:2, :] += dkg
        sq_ref[0, 0:1, :] += dqa
        sk_ref[0, 0:1, :] += dkva

    stat = lambda w: pl.BlockSpec((1, SUBLANE, w), lambda i: (i // tps, 0, 0))
    return _pcall(body, name=name, grid=(T // tm,),
                  in_specs=[_row_spec(tm, NH * LANE), _row_spec(tm, NH * LANE), _row_spec(tm, NH * NOPE),
                            _row_spec(tm, NH * LANE), _row_spec(tm, NH * LANE + NH * NOPE), _row_spec(tm, LAT_W),
                            whole(w_q), whole(w_kv), whole(qa_gain), whole(kva_gain), slab, slab, vec, vec],
                  out_specs=[_row_spec(tm, NH * LANE), _row_spec(tm, NH * LANE + NH * NOPE), slab, _row_spec(tm, QL),
                             _row_spec(tm, KVL), stat(LANE), stat(QL), stat(KVL)],
                  out_shape=[_sds((T, NH * LANE), MX), _sds((T, NH * LANE + NH * NOPE), MX), _sds((T, LANE), MX),
                             _sds((T, QL), MX), _sds((T, KVL), MX), _sds((nb, SUBLANE, LANE), F32),
                             _sds((nb, SUBLANE, QL), F32), _sds((nb, SUBLANE, KVL), F32)],
                  )(dq, dk, dv, qp, kvp, z, w_q, w_kv, qa_gain, kva_gain, cos, sin, q_gain, k_gain)


def _lower_triangle(t, keys_first=False):
    key_axis = 0 if keys_first else 1
    return lax.broadcasted_iota(jnp.int32, (t, t), key_axis) <= lax.broadcasted_iota(jnp.int32, (t, t), 1 - key_axis)


def _attn_fwd(q, k, v, *, nb, name, tq=512, comm=None):
    T = q.shape[0]
    S = T // nb
    tq = _pick(S, tq)
    nq = S // tq
    tk = tq
    npair = NH // 2

    def body(q_ref, k_ref, v_ref, o_ref, lse_ref):
        qi = pl.program_id(2)
        lane = lax.broadcasted_iota(jnp.int32, (tq, LANE), 1)
        qs = [q_ref[:, hh * LANE:(hh + 1) * LANE] for hh in range(2)]

        def block(j, carry, diagonal):
            k0 = pl.multiple_of(j * tk, tk)
            vb = v_ref[pl.ds(k0, tk), :]
            vbs = [jnp.where(lane < NOPE, vb, jnp.ones_like(vb)), jnp.where(lane < NOPE, jnp.ones_like(vb), vb)]
            new = []
            for hh in range(2):
                m, acc = carry[hh]
                kb = k_ref[pl.ds(k0, tk), hh * LANE:(hh + 1) * LANE]
                s = lax.dot_general(qs[hh], kb, (((1,), (1,)), ((), ())), preferred_element_type=F32)
                if diagonal:
                    s = jnp.where(_lower_triangle(tq), s, NEG)
                m_new = jnp.maximum(m, jnp.max(s, axis=-1, keepdims=True))
                p = jnp.exp((s - m_new).astype(MX))
                acc = jnp.exp(m - m_new) * acc + jnp.dot(p, vbs[hh], preferred_element_type=F32)
                new.append((m_new, acc))
            return tuple(new)

        init = tuple((jnp.full((tq, 1), NEG, F32), jnp.zeros((tq, LANE), F32)) for _ in range(2))
        carry = lax.fori_loop(0, qi, lambda j, c: block(j, c, False), init)
        (m0, acc0), (m1, acc1) = block(qi, carry, True)
        l0, l1 = acc0[:, NOPE:NOPE + 1], acc1[:, 0:1]
        o_ref[...] = jnp.where(lane < NOPE, acc0 / l0, acc1 / l1).astype(o_ref.dtype)
        lse_ref[...] = jnp.where(lane < NOPE, m0 + jnp.log(l0), m1 + jnp.log(l1))

    return _pcall(body, name=name, grid=(nb, npair, nq),
                  in_specs=[pl.BlockSpec((tq, 2 * LANE), lambda b, p, i: (b * nq + i, p)),
                            pl.BlockSpec((S, 2 * LANE), lambda b, p, i: (b, p)),
                            pl.BlockSpec((S, LANE), lambda b, p, i: (b, p))],
                  out_specs=[pl.BlockSpec((tq, LANE), lambda b, p, i: (b * nq + i, p)),
                             pl.BlockSpec((tq, LANE), lambda b, p, i: (b * nq + i, p))],
                  out_shape=[_sds((T, NH * NOPE), MX), _sds((T, NH * NOPE), F32)], comm=comm)(q, k, v)


def _attn_bwd(q, k, v, o, lse, do, *, nb, name, tq=512, comm=None):
    T = q.shape[0]
    S = T // nb
    tq = _pick(S, tq)
    nq = S // tq
    tk = tq
    npair = NH // 2

    def body(q_ref, k_ref, v_ref, o_ref, lse_ref, do_ref, dq_ref, dk_ref, dv_ref, lse_t, delta_t, dq_t):
        lane = lax.broadcasted_iota(jnp.int32, (tq, LANE), 1)
        first = lane < NOPE
        dq_t[...] = jnp.zeros_like(dq_t)

        def rows_step(qi, _):
            q0 = pl.multiple_of(qi * tq, tq)
            prod = do_ref[pl.ds(q0, tq), :] * o_ref[pl.ds(q0, tq), :].astype(F32)
            d0 = jnp.sum(jnp.where(first, prod, 0.0), axis=-1, keepdims=True)
            d1 = jnp.sum(jnp.where(first, 0.0, prod), axis=-1, keepdims=True)
            delta_t[:, pl.ds(q0, tq)] = jnp.where(first, d0, d1).T
            lse_t[:, pl.ds(q0, tq)] = lse_ref[pl.ds(q0, tq), :].T
            return 0

        lax.fori_loop(0, nq, rows_step, 0)

        def kv_step(kj, _):
            k0 = pl.multiple_of(kj * tk, tk)
            kbs = [k_ref[pl.ds(k0, tk), hh * LANE:(hh + 1) * LANE] for hh in range(2)]
            kbts = [kb.T for kb in kbs]
            vb = v_ref[pl.ds(k0, tk), :]

            def q_block(qi, carry, diagonal):
                dk0, dk1, dv = carry
                dks = [dk0, dk1]
                q0 = pl.multiple_of(qi * tq, tq)
                dov = do_ref[pl.ds(q0, tq), :]
                for hh in range(2):
                    qh = q_ref[pl.ds(q0, tq), hh * LANE:(hh + 1) * LANE]
                    dob = jnp.where(first if hh == 0 else jnp.logical_not(first), dov, 0.0).astype(MX)
                    st = lax.dot_general(kbs[hh], qh, (((1,), (1,)), ((), ())), preferred_element_type=F32)
                    pt = jnp.exp((st - lse_t[hh * NOPE:hh * NOPE + 1, pl.ds(q0, tq)]).astype(MX))
                    if diagonal:
                        pt = jnp.where(_lower_triangle(tq, keys_first=True), pt, jnp.zeros_like(pt))
                    dpt = lax.dot_general(vb, dob, (((1,), (1,)), ((), ())), preferred_element_type=F32)
                    dst = pt * (dpt - delta_t[hh * NOPE:hh * NOPE + 1, pl.ds(q0, tq)]).astype(MX)
                    dks[hh] = dks[hh] + jnp.dot(dst, qh, preferred_element_type=F32)
                    dv = dv + jnp.dot(pt, dob, preferred_element_type=F32)
                    dq_t[hh * LANE:(hh + 1) * LANE, pl.ds(q0, tq)] += jnp.dot(kbts[hh], dst, preferred_element_type=F32)
                return dks[0], dks[1], dv

            zero = jnp.zeros((tk, LANE), F32)
            carry = q_block(kj, (zero, zero, zero), True)
            dk0, dk1, dv = lax.fori_loop(kj + 1, nq, lambda qi, c: q_block(qi, c, False), carry)
            dk_ref[pl.ds(k0, tk), 0:LANE] = dk0
            dk_ref[pl.ds(k0, tk), LANE:2 * LANE] = dk1
            dv_ref[pl.ds(k0, tk), :] = dv
            return 0

        lax.fori_loop(0, nq, kv_step, 0)
        for qi in range(nq):
            dq_ref[qi * tq:(qi + 1) * tq, :] = dq_t[:, qi * tq:(qi + 1) * tq].T

    pair256 = pl.BlockSpec((S, 2 * LANE), lambda b, p: (b, p))
    pair128 = pl.BlockSpec((S, LANE), lambda b, p: (b, p))
    return _pcall(body, name=name, grid=(nb, npair),
                  in_specs=[pair256, pair256, pair128, pair128, pair128, pair128],
                  out_specs=[pair256, pair256, pair128],
                  out_shape=[_sds((T, NH * LANE), F32), _sds((T, NH * LANE), F32), _sds((T, NH * NOPE), F32)],
                  scratch=[pltpu.VMEM((LANE, S), F32), pltpu.VMEM((LANE, S), F32), pltpu.VMEM((2 * LANE, S), F32)],
                  comm=comm)(q, k, v, o, lse, do)


def _run(plan, fn, *args, name, **kw):
    rider = plan.rider(name)
    if rider is None:
        return fn(*args, name=name, **kw)
    outs, landed = fn(*args, name=name, comm=rider, **kw)
    plan.landed(name, landed)
    return outs


def _layer_fwd_bwd(x, tgt, mod3, cos, sin, plan, P):
    nb = mod3.shape[0]
    W = plan.W
    h1, gu1, a1 = _run(plan, _ffn_up_first, x, mod3, P["norm_ffn1"], W["ffn1_in"], sub=0, name="ffn1_up_a")
    gu1, a1 = _run(plan, _ffn_up_second, h1, W["ffn1_in"], gu1, a1, name="ffn1_up_b")
    f1, x1, h2 = _run(plan, _mm_resid_norm, a1, W["ffn1_out"], x, mod3, P["norm_mix"], sub=1, coef=0.5, name="ffn1_out")
    z = _run(plan, _mm, h2, W["w_in"], mode="nt", out_dtype=MX, name="mix_in", tn=1664)
    pooled, mixed, scaled = _pool_fwd(z, W["pool_grp"], P["pool_scale"], nb=nb, name="pool_fwd")
    br_pool = _mm(scaled, W["pool_proj"], mode="nn", out_dtype=MX, name="pool_proj")
    qn, kvn, qp, kvp, q, k, v = _mla_in_fwd(z, W["q_up"], W["kv_up"], P["q_a_norm"], P["kv_a_norm"], cos, sin,
                                            P["q_gain"], P["k_gain"], name="mla_in_fwd")
    attn, lse = _run(plan, _attn_fwd, q, k, v, nb=nb, name="attn_fwd")
    br_mla, merged, mo, x2, h3 = _mix_out_fwd(z, br_pool, attn, W["mla_proj"], W["w_out"], x1, mod3, P["norm_ffn2"], sub=2,
                                              name="mix_out", tm=256)
    gu2, a2 = _ffn_up(h3, W["ffn2_in"], name="ffn2_up")
    dy, df2, st_fin, loss = _mm_loss(a2, W["ffn2_out"], x2, tgt, mod3, name="ffn2_out_loss")

    plan.grad("ffn2_out", _dw(a2, df2, name="d_ffn2_out"))
    dgu2 = _ffn_dact(df2, gu2, W["ffn2_out"], name="ffn2_dact")
    plan.grad("ffn2_in", _ffn_dw_in(h3, dgu2, name="d_ffn2_in"))
    dx2, dmo, st3 = _run(plan, _ffn_dh_norm, dgu2, W["ffn2_in"], x2, dy, mo, mod3, P["norm_ffn2"], sub=2, coef=1.0,
                         name="d_ffn2_h")
    plan.grad("w_out", _dw(merged, dmo, name="d_mix_out"))
    dbr_pool, dbr_mla, dgates, dattn = _mix_out_bwd(dmo, W["w_out"], z, br_pool, br_mla, W["mla_proj"], name="mix_out_bwd", tm=256)
    plan.grad("pool_proj", _dw(scaled, dbr_pool, name="d_pool_proj"))
    dscaled = _mm(dbr_pool, W["pool_proj"], mode="nt", out_dtype=F32, name="d_pool_scaled")
    du_pool, d_pool_grp, d_pool_scale = _pool_bwd(dscaled, mixed, pooled, W["pool_grp"], P["pool_scale"], nb=nb, name="pool_bwd")
    plan.grad("mla_proj", _dw(attn, dbr_mla, name="d_mla_proj"))
    dq, dk, dv = _run(plan, _attn_bwd, q, k, v, attn, lse, dattn, nb=nb, name="attn_bwd")
    dqp, dkvp, dkr, dql, dkvl, st_prep, st_q, st_kv = _mla_in_bwd(
        dq, dk, dv, qp, kvp, z, W["q_up"], W["kv_up"], P["q_a_norm"], P["kv_a_norm"], cos, sin, P["q_gain"], P["k_gain"],
        nb=nb, name="mla_in_bwd")
    plan.grad("q_up", _dw(qn, dqp, name="d_q_up"))
    plan.grad("kv_up", _dw(kvn, dkvp, name="d_kv_up"))
    dz = jnp.concatenate([du_pool, dql, dkvl, dkr, dgates], axis=1)
    plan.grad("w_in", _run(plan, _dw, h2, dz, name="d_mix_in", tm=256, tn=1664, out_t=True))
    dx1, df1, st2 = _run(plan, _mm_norm_bwd, dz, W["w_in"], x1, dx2, f1, mod3, P["norm_mix"], sub=1, coef=0.5, name="d_mix_h")
    plan.grad("ffn1_out", _run(plan, _dw, a1, df1, name="d_ffn1_out"))
    dgu1 = _run(plan, _ffn_dact, df1, gu1, W["ffn1_out"], name="ffn1_dact")
    half = x.shape[1] // 2
    plan.grad("ffn1_in@0", _run(plan, _ffn_dw_in, h1, dgu1, rows=(0, half), name="d_ffn1_in_a"))
    plan.grad("ffn1_in@1", _run(plan, _ffn_dw_in, h1, dgu1, rows=(half, half), name="d_ffn1_in_b"))
    dh1 = _run(plan, _ffn_dh, dgu1, W["ffn1_in"], name="d_ffn1_h")
    grad_x, st1 = _run(plan, _bwd_block, x, dh1, dx1, None, mod3, P["norm_ffn1"], sub=0, coef=0.0, name="bwd_norm1")

    return loss, grad_x, (st1, st2, st3, st_fin, st_q, st_kv, st_prep, d_pool_scale), d_pool_grp


def _padded(rows):
    return rows + -rows % (4 * SUBLANE)


def _pad_rows(a):
    pad = [(0, 0)] * a.ndim
    pad[-2] = (0, _padded(a.shape[-2]) - a.shape[-2])
    return jnp.pad(a, pad)


def _w_in_pieces(shard):
    runs = []
    for ref0, ref1, k0 in ((0, Z_KR, 0), (Z_KR, Z_KR + ROPE, Z_KR + KR_LANE), (Z_KR + ROPE, N_CHIP * shard, Z_GP)):
        for q in range(N_CHIP):
            lo, hi = max(ref0, q * shard), min(ref1, (q + 1) * shard)
            if lo < hi:
                runs.append((k0 + lo - ref0, q * _padded(shard) + lo - q * shard, hi - lo))
    return runs


def _w_in_stacked_to_kernel(st, shard):
    flat = st.reshape(-1, st.shape[2])
    parts, at = [], 0
    for k0, s0, n in sorted(_w_in_pieces(shard)):
        parts += [jnp.zeros((k0 - at, flat.shape[1]), st.dtype)] * (k0 > at) + [flat[s0:s0 + n]]
        at = k0 + n
    assert at == Z_W
    return jnp.concatenate(parts, axis=0)


def _w_in_kernel_to_stacked(gt, shard):
    parts, at = [], 0
    for s0, k0, n in sorted((s, k, n) for k, s, n in _w_in_pieces(shard)) + [(N_CHIP * _padded(shard), 0, 0)]:
        parts += [jnp.zeros((s0 - at, gt.shape[1]), gt.dtype)] * (s0 > at) + [gt[k0:k0 + n]] * (n > 0)
        at = s0 + n
    return jnp.concatenate(parts, axis=0).reshape(N_CHIP, _padded(shard), gt.shape[1])


def _q_up_to_kernel(w):
    k = w.shape[0]
    return jnp.pad(w.reshape(k, NH, NOPE + ROPE), ((0, 0), (0, 0), (0, LANE - NOPE - ROPE))).reshape(k, NH * LANE)


def _q_up_from_kernel(g):
    k = g.shape[0]
    return g.reshape(k, NH, LANE)[:, :, :NOPE + ROPE].reshape(k, NH * (NOPE + ROPE))


def _kv_up_to_kernel(w):
    k = w.shape[0]
    w3 = w.reshape(k, NH, 2 * NOPE)
    kpart = jnp.pad(w3[:, :, :NOPE], ((0, 0), (0, 0), (0, LANE - NOPE))).reshape(k, NH * LANE)
    return jnp.concatenate([kpart, w3[:, :, NOPE:].reshape(k, NH * NOPE)], axis=1)


def _kv_up_from_kernel(g):
    k = g.shape[0]
    kpart = g[:, :NH * LANE].reshape(k, NH, LANE)[:, :, :NOPE]
    vpart = g[:, NH * LANE:].reshape(k, NH, NOPE)
    return jnp.concatenate([kpart, vpart], axis=2).reshape(k, NH * 2 * NOPE)


def _gain_slab(nope, rope):
    return jnp.concatenate([nope, rope, jnp.zeros((1, LANE - NOPE - ROPE), nope.dtype)], axis=1)


def _rope_tables(positions):
    inv_freq = 10000.0 ** (-jnp.arange(0, ROPE, 2, dtype=F32) / ROPE)
    ang = positions.astype(F32)[:, None] * inv_freq
    ang = jnp.concatenate([ang, ang], axis=-1)
    t = positions.shape[0]
    cos = jnp.concatenate([jnp.ones((t, KR_LANE), F32), jnp.cos(ang), jnp.ones((t, LANE - KR_LANE - ROPE), F32)], axis=1)
    sin = jnp.concatenate([jnp.zeros((t, KR_LANE), F32), jnp.sin(ang), jnp.zeros((t, LANE - KR_LANE - ROPE), F32)], axis=1)
    return cos, sin


def _coords():
    return lax.axis_index("x"), lax.axis_index("y"), lax.axis_index("c")


HBM_SPEC = pl.BlockSpec(memory_space=pl.ANY)
VMEM_SPEC = pl.BlockSpec(memory_space=pltpu.VMEM)


CHIP_RELS = ((1, 0), (0, 1), (1, 1))


def _gather_comm(shards):
    n = len(shards)

    def ici(ins, outs, sems, a, j, x, y, cc):
        half = ins[a].shape[0] // 2
        mine = pl.ds(cc * half, half)
        dx, dy = CHIP_RELS[j]
        return pltpu.make_async_remote_copy(src_ref=ins[a].at[mine], dst_ref=outs[a].at[2 * x + y, mine],
                                            send_sem=sems[0].at[a, j], recv_sem=sems[1].at[a, j],
                                            device_id=(x ^ dx, y ^ dy, cc), device_id_type=MESH)

    def d2d(ins, outs, sems, a, j, x, y, cc, half_of):
        half = ins[a].shape[0] // 2
        dx, dy = CHIP_RELS[j]
        landed = outs[a].at[2 * (x ^ dx) + (y ^ dy), pl.ds(half_of * half, half)]
        return pltpu.make_async_remote_copy(src_ref=landed, dst_ref=landed, send_sem=sems[2].at[a, j], recv_sem=sems[3].at[a, j],
                                            device_id=(x, y, 1 - cc), device_id_type=MESH)

    def own(ins, outs, sems, a, x, y):
        return pltpu.make_async_copy(ins[a], outs[a].at[2 * x + y], sems[4].at[a])

    def start(ins, outs, sems):
        x, y, cc = _coords()
        for a in range(n):
            own(ins, outs, sems, a, x, y).start()
            for j in range(3):
                ici(ins, outs, sems, a, j, x, y, cc).start()

    def finish(ins, outs, sems):
        x, y, cc = _coords()
        for a in range(n):
            for j in range(3):
                ici(ins, outs, sems, a, j, x, y, cc).wait_recv()
                d2d(ins, outs, sems, a, j, x, y, cc, cc).start()
        for a in range(n):
            for j in range(3):
                d2d(ins, outs, sems, a, j, x, y, cc, 1 - cc).wait_recv()
        for a in range(n):
            for j in range(3):
                ici(ins, outs, sems, a, j, x, y, cc).wait_send()
                d2d(ins, outs, sems, a, j, x, y, cc, cc).wait_send()
            own(ins, outs, sems, a, x, y).wait()

    dma = pltpu.SemaphoreType.DMA
    return _Comm(shards, [_sds((N_CHIP,) + s.shape, s.dtype) for s in shards],
                 [dma((n, 3)), dma((n, 3)), dma((n, 3)), dma((n, 3)), dma((n,))], start, finish)


def _swap_comm(parts):
    n = len(parts)

    def copy(ins, outs, sems, a):
        x, y, cc = _coords()
        half = ins[a].shape[1] // 2
        return pltpu.make_async_remote_copy(src_ref=ins[a].at[:, pl.ds((1 - cc) * half, half)], dst_ref=outs[a],
                                            send_sem=sems[0].at[a], recv_sem=sems[1].at[a], device_id=(x, y, 1 - cc),
                                            device_id_type=MESH)

    def start(ins, outs, sems):
        for a in range(n):
            copy(ins, outs, sems, a).start()

    def finish(ins, outs, sems):
        for a in range(n):
            copy(ins, outs, sems, a).wait()

    dma = pltpu.SemaphoreType.DMA
    return _Comm(parts, [_sds((p.shape[0], p.shape[1] // 2, p.shape[2]), p.dtype) for p in parts], [dma((n,)), dma((n,))],
                 start, finish)


def _add_half(full, other, cidx, *, name):
    nch, r, c = full.shape
    half = r // 2
    tr = _pick_rows(half)
    nbk = half // tr
    grid_spec = pltpu.PrefetchScalarGridSpec(
        num_scalar_prefetch=1, grid=(nch, nbk),
        in_specs=[pl.BlockSpec((1, tr, c), lambda j, i, cref: (j, cref[0] * nbk + i, 0)),
                  pl.BlockSpec((1, tr, c), lambda j, i, cref: (j, i, 0))],
        out_specs=pl.BlockSpec((1, tr, c), lambda j, i, cref: (j, i, 0)))

    def body(cref, a_ref, b_ref, o_ref):
        o_ref[...] = (a_ref[...] + b_ref[...]).astype(o_ref.dtype)

    return _pcall(body, name=name, out_shape=_sds((nch, half, c), MX), grid_spec=grid_spec)(cidx, full, other)


def _pick_rows(rows, target=512):
    best = None
    for t in range(16, min(rows, target) + 1, 16):
        if rows % t == 0:
            best = t
    return rows if best is None else best


def _exchange_comm(parts):
    n = len(parts)

    def send(ins, outs, sems, a, j, x, y, cc):
        dx, dy = CHIP_RELS[j]
        return pltpu.make_async_remote_copy(src_ref=ins[a].at[2 * (x ^ dx) + (y ^ dy)], dst_ref=outs[a].at[2 * x + y],
                                            send_sem=sems[0].at[a, j], recv_sem=sems[1].at[a, j],
                                            device_id=(x ^ dx, y ^ dy, cc), device_id_type=MESH)

    def landing(ins, outs, sems, a, j, x, y, cc):
        dx, dy = CHIP_RELS[j]
        peer_chip = 2 * (x ^ dx) + (y ^ dy)
        return pltpu.make_async_remote_copy(src_ref=ins[a].at[peer_chip], dst_ref=outs[a].at[peer_chip], send_sem=sems[0].at[a, j],
                                            recv_sem=sems[1].at[a, j], device_id=(x, y, cc), device_id_type=MESH)

    def own(ins, outs, sems, a, x, y):
        return pltpu.make_async_copy(ins[a].at[2 * x + y], outs[a].at[2 * x + y], sems[2].at[a])

    def start(ins, outs, sems):
        x, y, cc = _coords()
        for a in range(n):
            own(ins, outs, sems, a, x, y).start()
            for j in range(3):
                send(ins, outs, sems, a, j, x, y, cc).start()

    def finish(ins, outs, sems):
        x, y, cc = _coords()
        for a in range(n):
            for j in range(3):
                landing(ins, outs, sems, a, j, x, y, cc).wait_recv()
        for a in range(n):
            for j in range(3):
                send(ins, outs, sems, a, j, x, y, cc).wait_send()
            own(ins, outs, sems, a, x, y).wait()

    dma = pltpu.SemaphoreType.DMA
    return _Comm(parts, [_sds(p.shape, p.dtype) for p in parts], [dma((n, 3)), dma((n, 3)), dma((n,))], start, finish)


def _sum_chips(q, cidx, *, name):
    nch, h, c = q.shape
    tr = _pick_rows(h)
    nbk = h // tr
    grid_spec = pltpu.PrefetchScalarGridSpec(
        num_scalar_prefetch=1, grid=(nbk,),
        in_specs=[pl.BlockSpec((nch, tr, c), lambda i, cref: (0, i, 0))],
        out_specs=pl.BlockSpec((tr, c), lambda i, cref: (cref[0] * nbk + i, 0)))

    def body(cref, q_ref, o_ref):
        acc = q_ref[0].astype(F32) + q_ref[1].astype(F32)
        acc = acc + q_ref[2].astype(F32)
        o_ref[...] = acc + q_ref[3].astype(F32)

    return _pcall(body, name=name, out_shape=_sds((2 * h, c), F32), grid_spec=grid_spec)(cidx, q)


def _join_comm(fulls):
    n = len(fulls)

    def half_copy(outs, sems, a, which):
        x, y, cc = _coords()
        h = outs[a].shape[0] // 2
        rows = outs[a].at[pl.ds((cc if which == 0 else 1 - cc) * h, h)]
        return pltpu.make_async_remote_copy(src_ref=rows, dst_ref=rows, send_sem=sems[0].at[a], recv_sem=sems[1].at[a],
                                            device_id=(x, y, 1 - cc), device_id_type=MESH)

    def start(ins, outs, sems):
        for a in range(n):
            half_copy(outs, sems, a, 0).start()

    def finish(ins, outs, sems):
        for a in range(n):
            half_copy(outs, sems, a, 1).wait_recv()
        for a in range(n):
            half_copy(outs, sems, a, 0).wait_send()

    dma = pltpu.SemaphoreType.DMA
    return _Comm(fulls, [_sds(p.shape, p.dtype) for p in fulls], [dma((n,)), dma((n,))], start, finish,
                 aliases={a: a for a in range(n)})


def _ada_prologue(c, w, b, *, name, comm=None):
    nb, dm = c.shape
    n = w.shape[1]

    def body(c_ref, w_ref, b_ref, call_ref, land_ref, cpad, mod_s, g_send, g_recv, m_send, m_recv):
        x, y, cc = _coords()
        me = 4 * x + 2 * y + cc
        chip = 2 * x + y
        cpad[...] = jnp.zeros_like(cpad)
        cpad[0:nb, :] = c_ref[...]
        copies = []
        for kk in range(1, N_DEV):
            peer = (x ^ (kk >> 2), y ^ ((kk >> 1) & 1), cc ^ (kk & 1))
            cp = pltpu.make_async_remote_copy(src_ref=cpad, dst_ref=call_ref.at[me], send_sem=g_send.at[kk - 1],
                                              recv_sem=g_recv.at[kk - 1], device_id=peer, device_id_type=MESH)
            cp.start()
            copies.append(cp)
        call_ref[me] = cpad[...]
        for kk in range(1, N_DEV):
            pltpu.make_async_remote_copy(src_ref=cpad, dst_ref=call_ref.at[me ^ kk], send_sem=g_send.at[kk - 1],
                                         recv_sem=g_recv.at[kk - 1], device_id=(x, y, cc), device_id_type=MESH).wait_recv()
        cv = call_ref[...].reshape(N_DEV * SUBLANE, dm)
        act = (cv * _sigmoid(cv)).astype(MX)
        mod = jnp.dot(act, w_ref[...].astype(MX), preferred_element_type=F32) + b_ref[...]
        mod_s[...] = mod.reshape(N_DEV, SUBLANE, n)
        for j, (dx, dy) in enumerate(CHIP_RELS):
            cp = pltpu.make_async_remote_copy(src_ref=mod_s.at[4 * (x ^ dx) + 2 * (y ^ dy) + cc], dst_ref=land_ref.at[chip],
                                              send_sem=m_send.at[j], recv_sem=m_recv.at[j],
                                              device_id=(x ^ dx, y ^ dy, cc), device_id_type=MESH)
            cp.start()
            copies.append(cp)
        land_ref[chip] = mod_s[me]
        for j, (dx, dy) in enumerate(CHIP_RELS):
            pltpu.make_async_remote_copy(src_ref=mod_s.at[me], dst_ref=land_ref.at[2 * (x ^ dx) + (y ^ dy)], send_sem=m_send.at[j],
                                         recv_sem=m_recv.at[j], device_id=(x, y, cc), device_id_type=MESH).wait_recv()
        for cp in copies:
            cp.wait_send()

    dma = pltpu.SemaphoreType.DMA
    return _pcall(body, name=name, in_specs=[VMEM_SPEC] * 3, out_specs=[VMEM_SPEC] * 2,
                  out_shape=[_sds((N_DEV, SUBLANE, dm), F32), _sds((N_CHIP, SUBLANE, n), F32)],
                  scratch=[pltpu.VMEM((SUBLANE, dm), F32), pltpu.VMEM((N_DEV, SUBLANE, n), F32),
                           dma((N_DEV - 1,)), dma((N_DEV - 1,)), dma((3,)), dma((3,))], comm=comm)(c, w, b)


def _ada_bwd(c_all, dmod_cols, *, name):
    m, kdim = c_all.shape
    n = dmod_cols.shape[1]
    tn = _pick(n, 1152)

    def body(c_ref, d_ref, o_ref):
        cv = c_ref[...]
        act = (cv * _sigmoid(cv)).astype(MX)
        o_ref[...] = lax.dot_general(act, d_ref[...].astype(MX), (((0,), (0,)), ((), ())), preferred_element_type=F32)

    return _pcall(body, name=name, out_shape=_sds((kdim, n), F32), grid=(n // tn,),
                  in_specs=[pl.BlockSpec((m, kdim), lambda j: (0, 0)), pl.BlockSpec((m, tn), lambda j: (0, j))],
                  out_specs=pl.BlockSpec((kdim, tn), lambda j: (0, j)))(c_all, dmod_cols)


SLAB_W = 1024
RED_ROWS = 8


LOSS_LANE = SLAB_W - LANE


def _pack_stats(st1, st2, st3, st_fin, st_q, st_kv, st_prep, d_pool_scale, loss8, *, name):
    nb = st1.shape[0]
    dm_rows = -(-9 * nb // SUBLANE) * SUBLANE

    def body(s1, s2, s3, sf, sq, skv, sp, sps, loss_ref, o_ref):
        o_ref[...] = jnp.zeros_like(o_ref)
        for s in range(nb):
            rows = [s1[s, 0:1, :], s1[s, 1:2, :], s2[s, 3:4, :], s2[s, 0:1, :], s2[s, 1:2, :], s3[s, 3:4, :], s3[s, 0:1, :],
                    s3[s, 1:2, :], sf[s, 0:1, :]]
            for k, row in enumerate(rows):
                o_ref[9 * s + k:9 * s + k + 1, :] = row

        def over_seq(ref, r):
            acc = ref[0, r:r + 1, :]
            for s in range(1, nb):
                acc = acc + ref[s, r:r + 1, :]
            return acc

        o_ref[dm_rows + 0:dm_rows + 1, :] = over_seq(s1, 2)
        o_ref[dm_rows + 1:dm_rows + 2, :] = over_seq(s2, 2)
        o_ref[dm_rows + 2:dm_rows + 3, :] = over_seq(s3, 2)
        o_ref[dm_rows + 3:dm_rows + 4, 0:POOL_W] = over_seq(sps, 0)
        o_ref[dm_rows + 4:dm_rows + 5, 0:QL] = over_seq(sq, 0)
        o_ref[dm_rows + 5:dm_rows + 6, 0:KVL] = over_seq(skv, 0)
        o_ref[dm_rows + 6:dm_rows + 7, 0:LANE] = over_seq(sp, 0)
        o_ref[dm_rows + 7:dm_rows + 8, 0:LANE] = over_seq(sp, 1)
        o_ref[dm_rows + 7:dm_rows + 8, LOSS_LANE:] = loss_ref[0:1, :]

    return _pcall(body, name=name, out_shape=_sds((dm_rows + RED_ROWS, SLAB_W), F32), in_specs=[VMEM_SPEC] * 9,
                  out_specs=VMEM_SPEC)(st1, st2, st3, st_fin, st_q, st_kv, st_prep, d_pool_scale, loss8)


def _small_allreduce(slab, pool, *, name, comm=None):
    rows, w = slab.shape
    dm = rows - RED_ROWS
    prow, pw = pool.shape
    crow = RED_ROWS + 2 * dm

    def body(slab_ref, pool_ref, red_ref, dmod_ref, ptot_ref, sib_slab, sib_pool, chip_slab, chip_pool, land_slab, land_pool,
             a_send, a_recv, b_send, b_recv):
        x, y, cc = _coords()
        chip = 2 * x + y
        sib = (x, y, 1 - cc)
        to_sib = [pltpu.make_async_remote_copy(src_ref=slab_ref, dst_ref=sib_slab, send_sem=a_send.at[0], recv_sem=a_recv.at[0],
                                               device_id=sib, device_id_type=MESH),
                  pltpu.make_async_remote_copy(src_ref=pool_ref, dst_ref=sib_pool, send_sem=a_send.at[1], recv_sem=a_recv.at[1],
                                               device_id=sib, device_id_type=MESH)]
        for cp in to_sib:
            cp.start()
        for cp in to_sib:
            cp.wait()
        mine_dm, theirs_dm = slab_ref[0:dm, :], sib_slab[0:dm, :]
        chip_slab[0:RED_ROWS, :] = slab_ref[dm:, :] + sib_slab[dm:, :]
        chip_slab[RED_ROWS:RED_ROWS + dm, :] = jnp.where(cc == 0, mine_dm, theirs_dm)
        chip_slab[RED_ROWS + dm:, :] = jnp.where(cc == 0, theirs_dm, mine_dm)
        chip_pool[...] = pool_ref[...] + sib_pool[...]

        sends = []
        for j, (dx, dy) in enumerate(CHIP_RELS):
            peer = (x ^ dx, y ^ dy, cc)
            sends.append(pltpu.make_async_remote_copy(src_ref=chip_slab, dst_ref=land_slab.at[chip], send_sem=b_send.at[j, 0],
                                                      recv_sem=b_recv.at[j, 0], device_id=peer, device_id_type=MESH))
            sends.append(pltpu.make_async_remote_copy(src_ref=chip_pool, dst_ref=land_pool.at[chip], send_sem=b_send.at[j, 1],
                                                      recv_sem=b_recv.at[j, 1], device_id=peer, device_id_type=MESH))
        for cp in sends:
            cp.start()
        land_slab[chip] = chip_slab[...]
        land_pool[chip] = chip_pool[...]
        for j, (dx, dy) in enumerate(CHIP_RELS):
            peer_chip = 2 * (x ^ dx) + (y ^ dy)
            pltpu.make_async_remote_copy(src_ref=chip_slab, dst_ref=land_slab.at[peer_chip], send_sem=b_send.at[j, 0],
                                         recv_sem=b_recv.at[j, 0], device_id=(x, y, cc), device_id_type=MESH).wait_recv()
            pltpu.make_async_remote_copy(src_ref=chip_pool, dst_ref=land_pool.at[peer_chip], send_sem=b_send.at[j, 1],
                                         recv_sem=b_recv.at[j, 1], device_id=(x, y, cc), device_id_type=MESH).wait_recv()
        red = land_slab[0, 0:RED_ROWS, :]
        ptot = land_pool[0]
        for ch in range(1, N_CHIP):
            red = red + land_slab[ch, 0:RED_ROWS, :]
            ptot = ptot + land_pool[ch]
        red_ref[...] = red
        ptot_ref[...] = ptot
        for ch in range(N_CHIP):
            dmod_ref[2 * dm * ch:2 * dm * (ch + 1), :] = land_slab[ch, RED_ROWS:, :]
        for cp in sends:
            cp.wait_send()

    dma = pltpu.SemaphoreType.DMA
    return _pcall(body, name=name, in_specs=[VMEM_SPEC, VMEM_SPEC], out_specs=[VMEM_SPEC] * 3,
                  out_shape=[_sds((RED_ROWS, w), F32), _sds((N_DEV * dm, w), F32), _sds((prow, pw), F32)],
                  scratch=[pltpu.VMEM((rows, w), F32), pltpu.VMEM((prow, pw), F32), pltpu.VMEM((crow, w), F32),
                           pltpu.VMEM((prow, pw), F32), pltpu.VMEM((N_CHIP, crow, w), F32), pltpu.VMEM((N_CHIP, prow, pw), F32),
                           dma((2,)), dma((2,)), dma((3, 2)), dma((3, 2))], comm=comm)(slab, pool)


def _adamw_math(w, g, m, v):
    mn = ADAM_B1 * m + (1.0 - ADAM_B1) * g
    vn = ADAM_B2 * v + (1.0 - ADAM_B2) * (g * g)
    bc1 = 1.0 / (1.0 - ADAM_B1 ** ADAM_STEP)
    bc2 = 1.0 / (1.0 - ADAM_B2 ** ADAM_STEP)
    return -ADAM_LR * ((mn * bc1) / (jnp.sqrt(vn * bc2) + ADAM_EPS) + ADAM_WD * w), mn, vn


def _small_update(red, dmod_all, pool_total, nb, params, *, name):
    names = list(SMALL)
    dm = dmod_all.shape[0] // N_DEV

    def grad_of(nm, red_ref, dmod_ref, ptot_ref):
        if nm == "b_ada":
            acc = None
            for d in range(N_DEV):
                for s in range(nb):
                    blk = dmod_ref[d * dm + 9 * s:d * dm + 9 * s + 9, :]
                    acc = blk if acc is None else acc + blk
            return jnp.concatenate([acc[k:k + 1, :] for k in range(9)], axis=1)
        if nm == "pool_grp":
            return ptot_ref[...]
        row, lo, n = {"norm_ffn1": (0, 0, D), "norm_mix": (1, 0, D), "norm_ffn2": (2, 0, D), "pool_scale": (3, 0, POOL_W),
                      "q_a_norm": (4, 0, QL), "kv_a_norm": (5, 0, KVL), "q_norm_nope": (6, 0, NOPE),
                      "q_norm_rope": (6, NOPE, ROPE), "k_norm_nope": (7, 0, NOPE), "k_norm_rope": (7, KR_LANE, ROPE)}[nm]
        return red_ref[row:row + 1, lo:lo + n]

    def body(*refs):
        red_ref, dmod_ref, ptot_ref = refs[:3]
        ins = refs[3:3 + 3 * len(names)]
        outs = refs[3 + 3 * len(names):]
        outs[4 * len(names)][...] = red_ref[RED_ROWS - 1:RED_ROWS, LOSS_LANE:]
        for i, nm in enumerate(names):
            g = grad_of(nm, red_ref, dmod_ref, ptot_ref)
            d, mn, vn = _adamw_math(ins[3 * i][...], g, ins[3 * i + 1][...], ins[3 * i + 2][...])
            outs[4 * i][...] = g
            outs[4 * i + 1][...] = d
            outs[4 * i + 2][...] = mn
            outs[4 * i + 3][...] = vn

    flat_in = [a for nm in names for a in params[nm]]
    out_shape = [_sds(params[nm][0].shape, F32) for nm in names for _ in range(4)] + [_sds((1, LANE), F32)]
    res = _pcall(body, name=name, in_specs=[VMEM_SPEC] * (3 + len(flat_in)), out_specs=[VMEM_SPEC] * len(out_shape),
                 out_shape=out_shape)(red, dmod_all, pool_total, *flat_in)
    return {nm: tuple(res[4 * i:4 * i + 4]) for i, nm in enumerate(names)}, res[-1]


def _adamw(w, g, m, v, *, name, comm=None):
    r, c = w.shape
    tr = _pick_rows(r, 256)
    tc = c if tr < r else _pick(c, 256)
    spec = pl.BlockSpec((tr, tc), lambda i, j: (i, j))
    bc1 = 1.0 / (1.0 - ADAM_B1 ** ADAM_STEP)
    bc2 = 1.0 / (1.0 - ADAM_B2 ** ADAM_STEP)

    def body(w_ref, g_ref, m_ref, v_ref, d_ref, mo_ref, vo_ref):
        gv = g_ref[...]
        mn = ADAM_B1 * m_ref[...] + (1.0 - ADAM_B1) * gv
        vn = ADAM_B2 * v_ref[...] + (1.0 - ADAM_B2) * (gv * gv)
        mo_ref[...] = mn
        vo_ref[...] = vn
        d_ref[...] = -ADAM_LR * ((mn * bc1) / (jnp.sqrt(vn * bc2) + ADAM_EPS) + ADAM_WD * w_ref[...])

    out = _sds((r, c), F32)
    return _pcall(body, name=name, out_shape=[out, out, out], grid=(r // tr, c // tc), in_specs=[spec] * 4, out_specs=[spec] * 3,
                  comm=comm)(w, g, m, v)


BIG = ("w_ffn1_in", "w_ffn1_out", "w_in", "w_pool_proj", "w_q_up", "w_kv_up", "w_mla_proj", "w_out", "w_ffn2_in", "w_ffn2_out")
ROW_SHARDED = ("w_ffn1_out", "w_out", "w_ffn2_out")
KERNEL_NAME = {"w_ffn1_in": "ffn1_in", "w_ffn1_out": "ffn1_out", "w_in": "w_in", "w_pool_proj": "pool_proj", "w_q_up": "q_up",
               "w_kv_up": "kv_up", "w_mla_proj": "mla_proj", "w_out": "w_out", "w_ffn2_in": "ffn2_in", "w_ffn2_out": "ffn2_out"}
WEIGHTS = ("w_ada", "b_ada", "norm_ffn1", "w_ffn1_in", "w_ffn1_out", "norm_mix", "w_in", "pool_grp", "pool_scale", "w_pool_proj",
           "q_a_norm", "w_q_up", "kv_a_norm", "w_kv_up", "q_norm_nope", "q_norm_rope", "k_norm_nope", "k_norm_rope", "w_mla_proj",
           "w_out", "norm_ffn2", "w_ffn2_in", "w_ffn2_out")
SMALL = ("b_ada", "norm_ffn1", "norm_mix", "pool_grp", "pool_scale", "q_a_norm", "kv_a_norm", "q_norm_nope", "q_norm_rope",
         "k_norm_nope", "k_norm_rope", "norm_ffn2")


def _assemble(name, stacked):
    if name in ROW_SHARDED:
        return stacked.reshape(stacked.shape[0] * stacked.shape[1], stacked.shape[2])
    return jnp.transpose(stacked, (1, 0, 2)).reshape(stacked.shape[1], stacked.shape[0] * stacked.shape[2])


def _split(name, full):
    if name in ROW_SHARDED:
        return full.reshape(N_CHIP, full.shape[0] // N_CHIP, full.shape[1])
    return jnp.transpose(full.reshape(full.shape[0], N_CHIP, full.shape[1] // N_CHIP), (1, 0, 2))


GU = ("w_ffn1_in", "w_ffn2_in")
NARROW = ("w_in", "w_q_up")
TRANSPOSED = ("w_in",)
W_IN_SHARD = (Z_W - LANE + ROPE) // N_CHIP


MIX_SMALL = ("w_out", "w_pool_proj", "w_mla_proj", "w_q_up", "w_kv_up")
RIDES = {
    "ada_prologue": (("gather", ("w_ffn1_in",)),),
    "ffn1_up_a": (("gather", ("w_ffn1_out",)),),
    "ffn1_up_b": (("gather", ("w_in",)),),
    "mix_in": (("gather", MIX_SMALL),),
    "attn_fwd": (("gather", ("w_ffn2_in", "w_ffn2_out")),),
    "d_ffn2_h": (("swap", ("w_ffn2_out", "w_ffn2_in")),),
    "attn_bwd": (("exchange", ("w_ffn2_out", "w_ffn2_in")),),
    "d_mix_in": (("swap", MIX_SMALL),),
    "d_mix_h": (("exchange", MIX_SMALL), ("swap", ("w_in",))),
    "ffn1_dact": (("exchange", ("w_in",)), ("swap", ("w_ffn1_out",))),
    "d_ffn1_in_a": (("exchange", ("w_ffn1_out",)),),
    "d_ffn1_in_b": (("swap", ("w_ffn1_in@0",)),),
    "d_ffn1_h": (("exchange", ("w_ffn1_in@0",)), ("swap", ("w_ffn1_in@1",))),
    "bwd_norm1": (("exchange", ("w_ffn1_in@1",)),),
    "small_allreduce": (("join", tuple(n for n in BIG if n != "w_ffn1_in") + ("w_ffn1_in@0", "w_ffn1_in@1")),),
}


def _kname(n):
    base, _, part = n.partition("@")
    return KERNEL_NAME[base] + ("@" + part if part else "")


def _both(comms):
    if len(comms) == 1:
        return comms[0]
    ins, outs, sems, aliases, spans = [], [], [], {}, []
    for c in comms:
        spans.append((len(ins), len(c.ins), len(outs), len(c.out_shapes), len(sems), len(c.sems)))
        aliases.update({len(ins) + i: len(outs) + o for i, o in c.aliases.items()})
        ins, outs, sems = ins + c.ins, outs + c.out_shapes, sems + c.sems

    def each(which):
        def run(i_, o_, s_):
            for c, (ia, ni, oa, no, sa, ns) in zip(comms, spans):
                getattr(c, which)(i_[ia:ia + ni], o_[oa:oa + no], s_[sa:sa + ns])
        return run

    return _Comm(ins, outs, sems, each("start"), each("finish"), aliases)


class _ExchangePlan:
    def __init__(self, shards, cidx):
        self.shards, self.cidx = shards, cidx
        self.W, self.G, self.parts, self.pre, self.reduced, self.joined = {}, {}, {}, {}, {}, {}

    def grad(self, key, g):
        self.G[key] = g

    def rider(self, name):
        comms = []
        for kind, names in RIDES.get(name, ()):
            if kind == "gather":
                comms.append(_gather_comm([self.shards[n] for n in names]))
            elif kind == "swap":
                for n in names:
                    self.parts[n] = self._stacked(n)
                comms.append(_swap_comm([self.parts[n] for n in names]))
            elif kind == "exchange":
                comms.append(_exchange_comm([self.pre[n] for n in names]))
            else:
                comms.append(_join_comm([self.reduced[n] for n in names]))
        return _both(comms) if comms else None

    def landed(self, name, outs):
        at = 0
        for kind, names in RIDES[name]:
            for n, o in zip(names, outs[at:at + len(names)]):
                if kind == "gather":
                    self.W[KERNEL_NAME[n]] = self._to_kernel(n, o)
                elif kind == "swap":
                    self.pre[n] = _add_half(self.parts[n], o, self.cidx, name="rs_add_" + _kname(n))
                elif kind == "exchange":
                    self.reduced[n] = _sum_chips(o, self.cidx, name="rs_sum_" + _kname(n))
                else:
                    self.joined[n] = o
            at += len(names)

    @staticmethod
    def _to_kernel(n, stacked):
        if n in GU:
            return stacked
        if n in TRANSPOSED:
            return _w_in_stacked_to_kernel(stacked, W_IN_SHARD)
        full = _assemble(n, stacked)
        return {"w_q_up": _q_up_to_kernel, "w_kv_up": _kv_up_to_kernel}.get(n, lambda w: w)(full)

    def _stacked(self, n):
        g = self.G[_kname(n)]
        if n.partition("@")[0] in GU:
            return g
        if n in TRANSPOSED:
            return _w_in_kernel_to_stacked(g, W_IN_SHARD)
        full = {"w_q_up": _q_up_from_kernel, "w_kv_up": _kv_up_from_kernel}.get(n, lambda w: w)(g)
        return _split(n, full)


def kernel(x, c, positions, w_ada, b_ada, norm_ffn1, w_ffn1_in, w_ffn1_out, norm_mix, w_in, pool_grp, pool_scale, w_pool_proj, q_a_norm, w_q_up, kv_a_norm, w_kv_up, q_norm_nope, q_norm_rope, k_norm_nope, k_norm_rope, w_mla_proj, w_out, norm_ffn2, w_ffn2_in, w_ffn2_out, loss_target, m_w_ada, m_b_ada, m_norm_ffn1, m_w_ffn1_in, m_w_ffn1_out, m_norm_mix, m_w_in, m_pool_grp, m_pool_scale, m_w_pool_proj, m_q_a_norm, m_w_q_up, m_kv_a_norm, m_w_kv_up, m_q_norm_nope, m_q_norm_rope, m_k_norm_nope, m_k_norm_rope, m_w_mla_proj, m_w_out, m_norm_ffn2, m_w_ffn2_in, m_w_ffn2_out, v_w_ada, v_b_ada, v_norm_ffn1, v_w_ffn1_in, v_w_ffn1_out, v_norm_mix, v_w_in, v_pool_grp, v_pool_scale, v_w_pool_proj, v_q_a_norm, v_w_q_up, v_kv_a_norm, v_w_kv_up, v_q_norm_nope, v_q_norm_rope, v_k_norm_nope, v_k_norm_rope, v_w_mla_proj, v_w_out, v_norm_ffn2, v_w_ffn2_in, v_w_ffn2_out):
    args = dict(locals())
    wts = {n: args[n][0] for n in WEIGHTS}
    mom = {n: args["m_" + n][0] for n in WEIGHTS}
    var = {n: args["v_" + n][0] for n in WEIGHTS}
    nb, seq, dm = x.shape
    tokens = nb * seq
    xi, yi, ci = _coords()
    chip = 2 * xi + yi

    cidx = ci.astype(jnp.int32).reshape(1)
    plan = _ExchangePlan({n: _pad_rows(wts[n].T.astype(MX)) if n in TRANSPOSED else wts[n].astype(MX) for n in BIG}, cidx)
    plan.W["pool_grp"] = wts["pool_grp"].astype(MX)

    ncol = w_ada.shape[2]
    b_cols = lax.dynamic_slice_in_dim(wts["b_ada"].reshape(1, -1), chip * ncol, ncol, axis=1)
    c_slots, mod_slots = _run(plan, _ada_prologue, c, wts["w_ada"], b_cols, name="ada_prologue")
    c_all = c_slots[:, :nb].reshape(N_DEV * nb, dm)
    mod3 = jnp.transpose(mod_slots[:, :nb], (1, 0, 2)).reshape(nb, 9, dm)
    P = {"norm_ffn1": wts["norm_ffn1"].reshape(1, dm), "norm_mix": wts["norm_mix"].reshape(1, dm),
         "norm_ffn2": wts["norm_ffn2"].reshape(1, dm), "pool_scale": wts["pool_scale"].reshape(1, POOL_W),
         "q_a_norm": wts["q_a_norm"].reshape(1, QL), "kv_a_norm": wts["kv_a_norm"].reshape(1, KVL),
         "q_gain": _gain_slab(wts["q_norm_nope"].reshape(1, NOPE), wts["q_norm_rope"].reshape(1, ROPE)),
         "k_gain": _gain_slab(wts["k_norm_nope"].reshape(1, NOPE), wts["k_norm_rope"].reshape(1, ROPE))}
    cos, sin = _rope_tables(positions.reshape(tokens))

    loss8, grad_x, stats, d_pool_grp = _layer_fwd_bwd(x.reshape(tokens, dm), loss_target.reshape(tokens, dm), mod3, cos, sin, plan, P)

    slab = _pack_stats(*stats, loss8, name="pack_stats")
    red, dmod_rows, pool_total = _run(plan, _small_allreduce, slab, d_pool_grp.reshape(POOL_G * LANE, LANE), name="small_allreduce")
    grads_t = {n: plan.joined[n][:W_IN_SHARD] for n in TRANSPOSED}
    grads = {n: grads_t[n].T if n in TRANSPOSED else plan.joined[n] for n in BIG if n != "w_ffn1_in"}
    grads["w_ffn1_in"] = jnp.concatenate([plan.joined["w_ffn1_in@0"], plan.joined["w_ffn1_in@1"]], axis=0)
    dm_rows = dmod_rows.shape[0] // N_DEV
    dmod_all = dmod_rows.reshape(N_DEV, dm_rows, SLAB_W)[:, :9 * nb].reshape(N_DEV * nb, 9 * dm)
    dmod_cols = lax.dynamic_slice_in_dim(dmod_all, chip * ncol, ncol, axis=1)
    grads["w_ada"] = _ada_bwd(c_all, dmod_cols, name="ada_bwd")

    delta, new_m, new_v = {}, {}, {}
    as2d = lambda a: a.reshape(POOL_G * LANE, LANE) if a.ndim == 4 else a.reshape(1, -1)
    upd, loss_row = _small_update(red, dmod_rows, pool_total, nb,
                                  {n: tuple(as2d(args[p + n]) for p in ("", "m_", "v_")) for n in SMALL}, name="small_update")
    loss = loss_row[0, 0]
    for n in SMALL:
        grads[n], delta[n], new_m[n], new_v[n] = upd[n]
    for n in ("w_ada",) + BIG:
        if n in NARROW:
            res = _adamw(wts[n].T, grads_t[n] if n in TRANSPOSED else grads[n].T, mom[n].T, var[n].T, name="adamw_" + n)
            delta[n], new_m[n], new_v[n] = (r.T for r in res)
        else:
            delta[n], new_m[n], new_v[n] = _adamw(wts[n], grads[n], mom[n], var[n], name="adamw_" + n)

    def lead(a, n):
        return a.reshape((1,) + wts[n].shape)

    return (loss, grad_x.reshape(nb, seq, dm), *[lead(grads[n], n) for n in WEIGHTS], *[lead(delta[n], n) for n in WEIGHTS],
            *[lead(new_m[n], n) for n in WEIGHTS], *[lead(new_v[n], n) for n in WEIGHTS])
```

```python
import functools
import math

import jax
import jax.numpy as jnp
from jax import lax
from jax.experimental import pallas as pl
from jax.experimental.pallas import tpu as pltpu

F32 = jnp.float32
MX = jnp.bfloat16

D = 1024
DFF = 2816
NH = 8
POOL_W = 512
POOL_G = 4
QL = 384
KVL = 256
ROPE = 32
NOPE = 64
LANE = 128
SUBLANE = 8
EPS = 1e-6
ATTN_SCALE = 1.0 / math.sqrt(96.0)
NEG = -1e30

Z_UP, Z_QL, Z_KV, Z_KR, Z_GP, Z_GM, Z_W = 0, 512, 896, 1152, 1280, 2304, 3328
KR_LANE = 64
LAT_W = 1280

ADAM_LR, ADAM_B1, ADAM_B2, ADAM_EPS, ADAM_WD, ADAM_STEP = 0.001, 0.9, 0.999, 1e-08, 0.01, 10

VMEM_LIMIT = 48 * 1024 * 1024
MESH = pl.DeviceIdType.MESH
N_DEV = 8
N_CHIP = 4


class _Comm:
    def __init__(self, ins, out_shapes, sems, start, finish, aliases=None):
        self.ins, self.out_shapes, self.sems = list(ins), list(out_shapes), list(sems)
        self.start, self.finish = start, finish
        self.aliases = aliases or {}


def _pcall(body, *, name, out_shape, grid=(), in_specs=None, out_specs=None, scratch=(), grid_spec=None, aliases=None,
           comm=None):
    params = pltpu.CompilerParams(vmem_limit_bytes=VMEM_LIMIT)
    kw = dict(name=name, compiler_params=params)
    if comm is None:
        if aliases:
            kw["input_output_aliases"] = aliases
        if grid_spec is not None:
            return pl.pallas_call(body, grid_spec=grid_spec, out_shape=out_shape, **kw)
        return pl.pallas_call(body, grid=grid, in_specs=in_specs, out_specs=out_specs, scratch_shapes=scratch,
                              out_shape=out_shape, **kw)
    single = not isinstance(out_shape, (list, tuple))
    outs = [out_shape] if single else list(out_shape)
    ospecs = [out_specs] if single else list(out_specs)
    n_in, n_out, n_ci, n_co, n_scr = len(in_specs), len(outs), len(comm.ins), len(comm.out_shapes), len(scratch)
    io = dict(aliases or {})
    io.update({n_in + i: n_out + o for i, o in comm.aliases.items()})

    def riding(*refs):
        ins, cins = refs[:n_in], refs[n_in:n_in + n_ci]
        at = n_in + n_ci
        os_, couts = refs[at:at + n_out], refs[at + n_out:at + n_out + n_co]
        at += n_out + n_co
        scr, csems = refs[at:at + n_scr], refs[at + n_scr:]
        if grid:
            first = functools.reduce(jnp.logical_and, [pl.program_id(d) == 0 for d in range(len(grid))])
            last = functools.reduce(jnp.logical_and, [pl.program_id(d) == grid[d] - 1 for d in range(len(grid))])
            pl.when(first)(lambda: comm.start(cins, couts, csems))
            body(*ins, *os_, *scr)
            pl.when(last)(lambda: comm.finish(cins, couts, csems))
        else:
            comm.start(cins, couts, csems)
            body(*ins, *os_, *scr)
            comm.finish(cins, couts, csems)

    call = pl.pallas_call(riding, grid=grid, in_specs=list(in_specs) + [HBM_SPEC] * n_ci, out_specs=ospecs + [HBM_SPEC] * n_co,
                          out_shape=outs + comm.out_shapes, scratch_shapes=list(scratch) + comm.sems,
                          input_output_aliases=io, **kw)

    def run(*args):
        res = call(*args, *comm.ins)
        main = list(res[:n_out])
        return (main[0] if single else main), list(res[n_out:])

    return run


def _pick(dim, target):
    best = None
    for t in range(LANE, min(dim, target) + 1, LANE):
        if dim % t == 0:
            best = t
    return dim if best is None else best


def _sds(shape, dtype):
    return jax.ShapeDtypeStruct(shape, dtype)


def _dw(a, g, *, name, tm=512, tn=1024, out_t=False, comm=None):
    return _mm(a, g, mode="tn", out_dtype=F32, name=name, tm=tm, tn=tn, tk=a.shape[0], n_outer=True, out_t=out_t, comm=comm)


def _mm(a, b, *, mode, out_dtype, name, tm=1024, tn=1024, tk=4096, n_outer=False, out_t=False, comm=None):
    if mode == "nn":
        (M, K), (K2, N) = a.shape, b.shape
    elif mode == "nt":
        (M, K), (N, K2) = a.shape, b.shape
    else:
        (K, M), (K2, N) = a.shape, b.shape
    assert K == K2, (name, a.shape, b.shape)
    tm, tn, tk = _pick(M, tm), _pick(N, tn), _pick(K, tk)
    nk = K // tk
    if n_outer:
        ij = lambda g0, g1: (g1, g0)
        grid = (N // tn, M // tm, nk)
    else:
        ij = lambda g0, g1: (g0, g1)
        grid = (M // tm, N // tn, nk)
    if mode == "tn":
        a_spec = pl.BlockSpec((tk, tm), lambda g0, g1, k: (k, ij(g0, g1)[0]))
    else:
        a_spec = pl.BlockSpec((tm, tk), lambda g0, g1, k: (ij(g0, g1)[0], k))
    if mode == "nt":
        b_spec = pl.BlockSpec((tn, tk), lambda g0, g1, k: (ij(g0, g1)[1], k))
    else:
        b_spec = pl.BlockSpec((tk, tn), lambda g0, g1, k: (k, ij(g0, g1)[1]))
    if out_t:
        assert nk == 1, name
        o_spec = pl.BlockSpec((tn, tm), lambda g0, g1, k: ij(g0, g1)[::-1])
    else:
        o_spec = pl.BlockSpec((tm, tn), lambda g0, g1, k: ij(g0, g1))
    dn = {"nn": (((1,), (0,)), ((), ())), "nt": (((1,), (1,)), ((), ())), "tn": (((0,), (0,)), ((), ()))}[mode]

    def dot(a_ref, b_ref):
        return lax.dot_general(a_ref[...].astype(MX), b_ref[...].astype(MX), dn, preferred_element_type=F32)

    def body_one(a_ref, b_ref, o_ref):
        prod = dot(a_ref, b_ref)
        o_ref[...] = (prod.T if out_t else prod).astype(o_ref.dtype)

    def body_acc(a_ref, b_ref, o_ref, acc_ref):
        k = pl.program_id(2)
        part = dot(a_ref, b_ref)

        @pl.when(k == 0)
        def _():
            acc_ref[...] = part

        @pl.when(k > 0)
        def _():
            acc_ref[...] += part

        @pl.when(k == nk - 1)
        def _():
            o_ref[...] = acc_ref[...].astype(o_ref.dtype)

    return _pcall(body_one if nk == 1 else body_acc, name=name, out_shape=_sds((N, M) if out_t else (M, N), out_dtype), grid=grid,
                  in_specs=[a_spec, b_spec], out_specs=o_spec, scratch=[] if nk == 1 else [pltpu.VMEM((tm, tn), F32)],
                  comm=comm)(a, b)


def _gu_shard(q):
    return (q % 2) * 2 + q // 2


def _ffn_up(h, w_st, *, name, tm=512, comm=None):
    T, dm = h.shape
    hw = w_st.shape[2]
    tm = _pick(T, tm)

    def body(h_ref, wg_ref, wu_ref, gu_ref, a_ref):
        hv = h_ref[...]
        g = jnp.dot(hv, wg_ref[0], preferred_element_type=F32)
        u = jnp.dot(hv, wu_ref[0], preferred_element_type=F32)
        gu_ref[:, :hw] = g.astype(gu_ref.dtype)
        gu_ref[:, hw:] = u.astype(gu_ref.dtype)
        a_ref[...] = (g * _sigmoid(g) * u).astype(a_ref.dtype)

    return _pcall(body, name=name, grid=(T // tm, 2),
                  in_specs=[pl.BlockSpec((tm, dm), lambda i, j: (i, 0)), pl.BlockSpec((1, dm, hw), lambda i, j: (j, 0, 0)),
                            pl.BlockSpec((1, dm, hw), lambda i, j: (2 + j, 0, 0))],
                  out_specs=[pl.BlockSpec((tm, 2 * hw), lambda i, j: (i, j)), pl.BlockSpec((tm, hw), lambda i, j: (i, j))],
                  out_shape=[_sds((T, 4 * hw), MX), _sds((T, 2 * hw), MX)], comm=comm)(h, w_st, w_st)


def _ffn_up_first(x, mod3, gain, w_st, *, sub, name, tm=512, comm=None):
    T, dm = x.shape
    hw = w_st.shape[2]
    tm = _pick(T, tm)
    tps = (T // mod3.shape[0]) // tm

    def body(x_ref, mod_ref, n_ref, wg_ref, wu_ref, h_ref, gu_ref, a_ref):
        xv = x_ref[...]
        xn = xv * _rsq(xv) * n_ref[...]
        hv = (xn * (1.0 + mod_ref[0, 3 * sub + 1:3 * sub + 2, :]) + mod_ref[0, 3 * sub:3 * sub + 1, :]).astype(h_ref.dtype)
        h_ref[...] = hv
        g = jnp.dot(hv, wg_ref[0], preferred_element_type=F32)
        u = jnp.dot(hv, wu_ref[0], preferred_element_type=F32)
        gu_ref[:, :hw] = g.astype(gu_ref.dtype)
        gu_ref[:, hw:] = u.astype(gu_ref.dtype)
        a_ref[...] = (g * _sigmoid(g) * u).astype(a_ref.dtype)

    return _pcall(body, name=name, grid=(T // tm,),
                  in_specs=[_row_spec(tm, dm), pl.BlockSpec((1, 9, dm), lambda i: (i // tps, 0, 0)),
                            pl.BlockSpec((1, dm), lambda i: (0, 0)), pl.BlockSpec((1, dm, hw), lambda i: (0, 0, 0)),
                            pl.BlockSpec((1, dm, hw), lambda i: (2, 0, 0))],
                  out_specs=[_row_spec(tm, dm), pl.BlockSpec((tm, 2 * hw), lambda i: (i, 0)), pl.BlockSpec((tm, hw), lambda i: (i, 0))],
                  out_shape=[_sds((T, dm), MX), _sds((T, 4 * hw), MX), _sds((T, 2 * hw), MX)],
                  comm=comm)(x, mod3, gain, w_st, w_st)


def _ffn_up_second(h, w_st, gu, a, *, name, tm=512, comm=None):
    T, dm = h.shape
    hw = w_st.shape[2]
    tm = _pick(T, tm)

    def body(h_ref, wg_ref, wu_ref, gu_in, a_in, gu_ref, a_ref):
        hv = h_ref[...]
        g = jnp.dot(hv, wg_ref[0], preferred_element_type=F32)
        u = jnp.dot(hv, wu_ref[0], preferred_element_type=F32)
        gu_ref[:, :hw] = g.astype(gu_ref.dtype)
        gu_ref[:, hw:] = u.astype(gu_ref.dtype)
        a_ref[...] = (g * _sigmoid(g) * u).astype(a_ref.dtype)

    return _pcall(body, name=name, grid=(T // tm,),
                  in_specs=[_row_spec(tm, dm), pl.BlockSpec((1, dm, hw), lambda i: (1, 0, 0)),
                            pl.BlockSpec((1, dm, hw), lambda i: (3, 0, 0)), HBM_SPEC, HBM_SPEC],
                  out_specs=[pl.BlockSpec((tm, 2 * hw), lambda i: (i, 1)), pl.BlockSpec((tm, hw), lambda i: (i, 1))],
                  out_shape=[_sds(gu.shape, gu.dtype), _sds(a.shape, a.dtype)], aliases={3: 0, 4: 1},
                  comm=comm)(h, w_st, w_st, gu, a)


def _ffn_dact(df, gu, w_out, *, name, tm=512, comm=None):
    T, dm = df.shape
    hw = gu.shape[1] // 4
    tm = _pick(T, tm)

    def body(df_ref, gu_ref, wo_ref, dgu_ref):
        da = lax.dot_general(df_ref[...], wo_ref[...], (((1,), (1,)), ((), ())), preferred_element_type=F32)
        g = gu_ref[:, :hw].astype(F32)
        u = gu_ref[:, hw:].astype(F32)
        s = _sigmoid(g)
        dgu_ref[:, :hw] = (da * u * (s * (1.0 + g * (1.0 - s)))).astype(dgu_ref.dtype)
        dgu_ref[:, hw:] = (da * (g * s)).astype(dgu_ref.dtype)

    return _pcall(body, name=name, grid=(T // tm, 2),
                  in_specs=[pl.BlockSpec((tm, dm), lambda i, j: (i, 0)), pl.BlockSpec((tm, 2 * hw), lambda i, j: (i, j)),
                            pl.BlockSpec((hw, dm), lambda i, j: (j, 0))],
                  out_specs=pl.BlockSpec((tm, 2 * hw), lambda i, j: (i, j)), out_shape=_sds(gu.shape, MX),
                  comm=comm)(df, gu, w_out)


def _ffn_dh(dgu, w_st, *, name, tm=1024, comm=None):
    T = dgu.shape[0]
    _, dm, hw = w_st.shape
    tm = _pick(T, tm)

    def body(d_ref, w_ref, o_ref, acc_ref):
        q = pl.program_id(1)
        part = lax.dot_general(d_ref[...], w_ref[0], (((1,), (1,)), ((), ())), preferred_element_type=F32)

        @pl.when(q == 0)
        def _():
            acc_ref[...] = part

        @pl.when(jnp.logical_and(q > 0, q < 3))
        def _():
            acc_ref[...] += part

        @pl.when(q == 3)
        def _():
            o_ref[...] = acc_ref[...] + part

    return _pcall(body, name=name, grid=(T // tm, 4),
                  in_specs=[pl.BlockSpec((tm, hw), lambda i, q: (i, q)), pl.BlockSpec((1, dm, hw), lambda i, q: (_gu_shard(q), 0, 0))],
                  out_specs=pl.BlockSpec((tm, dm), lambda i, q: (i, 0)), out_shape=_sds((T, dm), F32),
                  scratch=[pltpu.VMEM((tm, dm), F32)], comm=comm)(dgu, w_st)


def _ffn_dw_in(h, dgu, *, name, rows=None, tm=512, comm=None):
    T, dm = h.shape
    hw = dgu.shape[1] // 4
    first, count = rows if rows is not None else (0, dm)
    tm = _pick(count, tm)
    skip = first // tm

    def body(h_ref, d_ref, o_ref):
        o_ref[0] = lax.dot_general(h_ref[...], d_ref[...], (((0,), (0,)), ((), ())), preferred_element_type=F32)

    return _pcall(body, name=name, grid=(4, count // tm),
                  in_specs=[pl.BlockSpec((T, tm), lambda q, i: (0, skip + i)), pl.BlockSpec((T, hw), lambda q, i: (0, q))],
                  out_specs=pl.BlockSpec((1, tm, hw), lambda q, i: (_gu_shard(q), i, 0)),
                  out_shape=_sds((4, count, hw), F32), comm=comm)(h, dgu)


def _rsq(x):
    return lax.rsqrt(jnp.mean(x * x, axis=-1, keepdims=True) + EPS)


def _sigmoid(x):
    return 1.0 / (1.0 + jnp.exp(-x))


def _row_spec(tm, w):
    return pl.BlockSpec((tm, w), lambda i: (i, 0))


def _mm_resid_norm(a, w, x_prev, mod3, gain, *, sub, coef, name, tm=512, comm=None):
    T, k = a.shape
    dm = w.shape[1]
    tm = _pick(T, tm)
    tps = (T // mod3.shape[0]) // tm

    def body(a_ref, w_ref, x_ref, mod_ref, n_ref, f_ref, xo_ref, h_ref):
        f = jnp.dot(a_ref[...], w_ref[...], preferred_element_type=F32)
        f_ref[...] = f
        x = x_ref[...] + coef * mod_ref[0, 3 * sub - 1:3 * sub, :] * f
        xo_ref[...] = x
        xn = x * _rsq(x) * n_ref[...]
        h_ref[...] = (xn * (1.0 + mod_ref[0, 3 * sub + 1:3 * sub + 2, :]) + mod_ref[0, 3 * sub:3 * sub + 1, :]).astype(h_ref.dtype)

    row = _row_spec(tm, dm)
    return _pcall(body, name=name, grid=(T // tm,),
                  in_specs=[_row_spec(tm, k), pl.BlockSpec((k, dm), lambda i: (0, 0)), row,
                            pl.BlockSpec((1, 9, dm), lambda i: (i // tps, 0, 0)), pl.BlockSpec((1, dm), lambda i: (0, 0))],
                  out_specs=[row, row, row], out_shape=[_sds((T, dm), F32), _sds((T, dm), F32), _sds((T, dm), MX)],
                  comm=comm)(a, w, x_prev, mod3, gain)


def _mm_loss(a, w, x2, tgt, mod3, *, name, tm=512):
    T, k = a.shape
    dm = w.shape[1]
    tm = _pick(T, tm)
    tps = (T // mod3.shape[0]) // tm
    mod_spec = pl.BlockSpec((1, 9, dm), lambda i: (i // tps, 0, 0))
    row = _row_spec(tm, dm)
    stat_spec = pl.BlockSpec((1, SUBLANE, dm), lambda i: (i // tps, 0, 0))
    loss_spec = pl.BlockSpec((SUBLANE, LANE), lambda i: (0, 0))

    def body(a_ref, w_ref, x_ref, t_ref, mod_ref, dy_ref, df_ref, st_ref, loss_ref):
        i = pl.program_id(0)
        g = mod_ref[0, 8:9, :]
        f = jnp.dot(a_ref[...], w_ref[...], preferred_element_type=F32)
        err = x_ref[...] + 0.5 * g * f - t_ref[...]
        dy = err * (1.0 / dm)
        dy_ref[...] = dy
        df_ref[...] = (0.5 * g * dy).astype(df_ref.dtype)
        dgate = jnp.sum(0.5 * dy * f, axis=0, keepdims=True)
        part = 0.5 * jnp.sum(jnp.sum(err * err, axis=0, keepdims=True), axis=1, keepdims=True) * (1.0 / dm)

        @pl.when(i % tps == 0)
        def _():
            st_ref[...] = jnp.zeros_like(st_ref)

        @pl.when(i == 0)
        def _():
            loss_ref[...] = jnp.zeros_like(loss_ref)

        st_ref[0, 0:1, :] += dgate
        loss_ref[...] += jnp.broadcast_to(part, loss_ref.shape)

    return _pcall(body, name=name, grid=(T // tm,),
                  in_specs=[_row_spec(tm, k), pl.BlockSpec((k, dm), lambda i: (0, 0)), row, row, mod_spec],
                  out_specs=[row, row, stat_spec, loss_spec],
                  out_shape=[_sds((T, dm), F32), _sds((T, dm), MX), _sds((mod3.shape[0], SUBLANE, dm), F32),
                             _sds((SUBLANE, LANE), F32)])(a, w, x2, tgt, mod3)


def _norm_bwd_tail(dhv, x_ref, dxi_ref, f_ref, mod_ref, n_ref, dx_ref, df_ref, st_ref, *, first, sub, coef):
    x = x_ref[...]
    r = _rsq(x)
    xhat = x * r
    n = n_ref[...]
    d_shift = jnp.sum(dhv, axis=0, keepdims=True)
    d_scale = jnp.sum(dhv * (xhat * n), axis=0, keepdims=True)
    dxn = dhv * (1.0 + mod_ref[0, 3 * sub + 1:3 * sub + 2, :])
    d_gain = jnp.sum(dxn * xhat, axis=0, keepdims=True)
    dxhat = dxn * n
    dx = dxi_ref[...] + r * (dxhat - xhat * jnp.mean(dxhat * xhat, axis=-1, keepdims=True))
    dx_ref[...] = dx

    @pl.when(first)
    def _():
        st_ref[...] = jnp.zeros_like(st_ref)

    st_ref[0, 0:1, :] += d_shift
    st_ref[0, 1:2, :] += d_scale
    st_ref[0, 2:3, :] += d_gain
    if f_ref is not None:
        st_ref[0, 3:4, :] += jnp.sum(coef * dx * f_ref[...], axis=0, keepdims=True)
        df_ref[...] = (coef * mod_ref[0, 3 * sub - 1:3 * sub, :] * dx).astype(df_ref.dtype)


def _ffn_dh_norm(dgu, w_st, x_cur, dx_in, f_prev, mod3, gain, *, sub, coef, name, tm=512, comm=None):
    T = dgu.shape[0]
    _, dm, hw = w_st.shape
    tm = _pick(T, tm)
    nb = mod3.shape[0]
    tps = (T // nb) // tm

    def body(d_ref, w_ref, x_ref, dxi_ref, f_ref, mod_ref, n_ref, dx_ref, df_ref, st_ref, acc_ref):
        i = pl.program_id(0)
        q = pl.program_id(1)
        part = lax.dot_general(d_ref[...], w_ref[0], (((1,), (1,)), ((), ())), preferred_element_type=F32)

        @pl.when(q == 0)
        def _():
            acc_ref[...] = part

        @pl.when(q > 0)
        def _():
            acc_ref[...] += part

        @pl.when(q == 3)
        def _():
            _norm_bwd_tail(acc_ref[...], x_ref, dxi_ref, f_ref, mod_ref, n_ref, dx_ref, df_ref, st_ref,
                           first=i % tps == 0, sub=sub, coef=coef)

    row = pl.BlockSpec((tm, dm), lambda i, q: (i, 0))
    return _pcall(body, name=name, grid=(T // tm, 4),
                  in_specs=[pl.BlockSpec((tm, hw), lambda i, q: (i, q)), pl.BlockSpec((1, dm, hw), lambda i, q: (_gu_shard(q), 0, 0)),
                            row, row, row, pl.BlockSpec((1, 9, dm), lambda i, q: (i // tps, 0, 0)),
                            pl.BlockSpec((1, dm), lambda i, q: (0, 0))],
                  out_specs=[row, row, pl.BlockSpec((1, SUBLANE, dm), lambda i, q: (i // tps, 0, 0))],
                  out_shape=[_sds((T, dm), F32), _sds((T, dm), MX), _sds((nb, SUBLANE, dm), F32)],
                  scratch=[pltpu.VMEM((tm, dm), F32)], comm=comm)(dgu, w_st, x_cur, dx_in, f_prev, mod3, gain)


def _mm_norm_bwd(a, b, x_cur, dx_in, f_prev, mod3, gain, *, sub, coef, name, tm=512, comm=None):
    T, k = a.shape
    dm = b.shape[1]
    tm = _pick(T, tm)
    nb = mod3.shape[0]
    tps = (T // nb) // tm

    def body(a_ref, b_ref, x_ref, dxi_ref, f_ref, mod_ref, n_ref, dx_ref, df_ref, st_ref):
        dh = jnp.dot(a_ref[...], b_ref[...], preferred_element_type=F32)
        _norm_bwd_tail(dh, x_ref, dxi_ref, f_ref, mod_ref, n_ref, dx_ref, df_ref, st_ref,
                       first=pl.program_id(0) % tps == 0, sub=sub, coef=coef)

    row = _row_spec(tm, dm)
    return _pcall(body, name=name, grid=(T // tm,),
                  in_specs=[_row_spec(tm, k), pl.BlockSpec((k, dm), lambda i: (0, 0)), row, row, row,
                            pl.BlockSpec((1, 9, dm), lambda i: (i // tps, 0, 0)), pl.BlockSpec((1, dm), lambda i: (0, 0))],
                  out_specs=[row, row, pl.BlockSpec((1, SUBLANE, dm), lambda i: (i // tps, 0, 0))],
                  out_shape=[_sds((T, dm), F32), _sds((T, dm), MX), _sds((nb, SUBLANE, dm), F32)],
                  comm=comm)(a, b, x_cur, dx_in, f_prev, mod3, gain)


def _bwd_block(x_cur, dh, dx_in, f_prev, mod3, gain, *, sub, coef, name, tm=256, comm=None):
    T, dm = x_cur.shape
    nb = mod3.shape[0]
    tps = (T // nb) // tm
    has_f = f_prev is not None
    mod_spec = pl.BlockSpec((1, 9, dm), lambda i: (i // tps, 0, 0))
    vec_spec = pl.BlockSpec((1, dm), lambda i: (0, 0))
    stat_spec = pl.BlockSpec((1, SUBLANE, dm), lambda i: (i // tps, 0, 0))
    row = _row_spec(tm, dm)

    def body(*refs):
        if has_f:
            x_ref, dh_ref, dxi_ref, f_ref, mod_ref, n_ref, dx_ref, df_ref, st_ref = refs
        else:
            x_ref, dh_ref, dxi_ref, mod_ref, n_ref, dx_ref, st_ref = refs
            f_ref = df_ref = None
        _norm_bwd_tail(dh_ref[...], x_ref, dxi_ref, f_ref, mod_ref, n_ref, dx_ref, df_ref, st_ref,
                       first=pl.program_id(0) % tps == 0, sub=sub, coef=coef)

    st_shape = _sds((nb, SUBLANE, dm), F32)
    if has_f:
        return _pcall(body, name=name, grid=(T // tm,), in_specs=[row, row, row, row, mod_spec, vec_spec],
                      out_specs=[row, row, stat_spec],
                      out_shape=[_sds((T, dm), F32), _sds((T, dm), MX), st_shape], comm=comm)(x_cur, dh, dx_in, f_prev, mod3, gain)
    return _pcall(body, name=name, grid=(T // tm,), in_specs=[row, row, row, mod_spec, vec_spec],
                  out_specs=[row, stat_spec], out_shape=[_sds((T, dm), F32), st_shape], comm=comm)(x_cur, dh, dx_in, mod3, gain)


def _mix_out_fwd(z, br_pool, attn, w_mla, w_out, x_prev, mod3, gain, *, sub, name, tm=512):
    T = z.shape[0]
    dm = w_out.shape[1]
    tm = _pick(T, tm)
    tps = (T // mod3.shape[0]) // tm
    whole = lambda a: pl.BlockSpec(a.shape, lambda i: (0, 0))

    def body(z_ref, bp_ref, at_ref, wm_ref, wo_ref, x_ref, mod_ref, n_ref, bm_ref, mg_ref, mo_ref, xo_ref, h_ref):
        bm = jnp.dot(at_ref[...], wm_ref[...], preferred_element_type=F32).astype(MX)
        bm_ref[...] = bm
        gp = z_ref[:, Z_GP:Z_GP + D].astype(F32)
        gm = z_ref[:, Z_GM:Z_GM + D].astype(F32)
        merged = (_sigmoid(gp) * bp_ref[...] + _sigmoid(gm) * bm).astype(MX)
        mg_ref[...] = merged
        mo = jnp.dot(merged, wo_ref[...], preferred_element_type=F32)
        mo_ref[...] = mo
        x = x_ref[...] + mod_ref[0, 3 * sub - 1:3 * sub, :] * mo
        xo_ref[...] = x
        xn = x * _rsq(x) * n_ref[...]
        h_ref[...] = (xn * (1.0 + mod_ref[0, 3 * sub + 1:3 * sub + 2, :]) + mod_ref[0, 3 * sub:3 * sub + 1, :]).astype(h_ref.dtype)

    row = _row_spec(tm, dm)
    return _pcall(body, name=name, grid=(T // tm,),
                  in_specs=[_row_spec(tm, Z_W), row, _row_spec(tm, attn.shape[1]), whole(w_mla), whole(w_out), row,
                            pl.BlockSpec((1, 9, dm), lambda i: (i // tps, 0, 0)), pl.BlockSpec((1, dm), lambda i: (0, 0))],
                  out_specs=[row, row, row, row, row],
                  out_shape=[_sds((T, dm), MX), _sds((T, dm), MX), _sds((T, dm), F32), _sds((T, dm), F32), _sds((T, dm), MX)],
                  )(z, br_pool, attn, w_mla, w_out, x_prev, mod3, gain)


def _mix_out_bwd(dmo, w_out, z, br_pool, br_mla, w_mla, *, name, tm=512):
    T = z.shape[0]
    tm = _pick(T, tm)
    whole = lambda a: pl.BlockSpec(a.shape, lambda i: (0, 0))
    nt = (((1,), (1,)), ((), ()))

    def body(dmo_ref, wo_ref, z_ref, bp_ref, bm_ref, wm_ref, dbp_ref, dbm_ref, dg_ref, dat_ref):
        dm = lax.dot_general(dmo_ref[...], wo_ref[...], nt, preferred_element_type=F32)
        sp = _sigmoid(z_ref[:, Z_GP:Z_GP + D].astype(F32))
        sm = _sigmoid(z_ref[:, Z_GM:Z_GM + D].astype(F32))
        dbp_ref[...] = (dm * sp).astype(dbp_ref.dtype)
        dbm = (dm * sm).astype(MX)
        dbm_ref[...] = dbm
        dg_ref[:, :D] = (dm * bp_ref[...].astype(F32) * sp * (1.0 - sp)).astype(dg_ref.dtype)
        dg_ref[:, D:] = (dm * bm_ref[...].astype(F32) * sm * (1.0 - sm)).astype(dg_ref.dtype)
        dat_ref[...] = lax.dot_general(dbm, wm_ref[...], nt, preferred_element_type=F32)

    row = _row_spec(tm, D)
    return _pcall(body, name=name, grid=(T // tm,),
                  in_specs=[row, whole(w_out), _row_spec(tm, Z_W), row, row, whole(w_mla)],
                  out_specs=[row, row, _row_spec(tm, 2 * D), _row_spec(tm, w_mla.shape[0])],
                  out_shape=[_sds((T, D), MX), _sds((T, D), MX), _sds((T, 2 * D), MX), _sds((T, w_mla.shape[0]), F32)],
                  )(dmo, w_out, z, br_pool, br_mla, w_mla)


def _shift_down(x, k, row):
    return jnp.where(row >= k, pltpu.roll(x, k, 0), 0.0)


def _shift_up(x, k, row, n):
    return jnp.where(row < n - k, pltpu.roll(x, n - k, 0), 0.0)


def _pool_fwd(z, pool_grp, pool_scale, *, nb, name):
    T = z.shape[0]
    S = T // nb
    blk = pl.BlockSpec((S, LANE), lambda b, g: (b, g))

    def body(u_ref, w_ref, s_ref, pooled_ref, mixed_ref, scaled_ref):
        g = pl.program_id(1)
        u = u_ref[...].astype(F32)
        row = lax.broadcasted_iota(jnp.int32, u.shape, 0)
        s2 = u + _shift_down(u, 1, row)
        s4 = s2 + _shift_down(s2, 2, row)
        s8 = s4 + _shift_down(s4, 4, row)
        s16 = s8 + _shift_down(s8, 8, row)
        win = jnp.where(g == 0, s2, jnp.where(g == 1, s4, jnp.where(g == 2, s8, s16)))
        width = lax.shift_left(jnp.int32(2), g)
        cnt = jnp.minimum(row + 1, width).astype(F32)
        pooled = (win / cnt - u).astype(MX)
        pooled_ref[...] = pooled
        mixed = jnp.dot(pooled, w_ref[0], preferred_element_type=F32)
        mixed_ref[...] = mixed
        scaled_ref[...] = (mixed * s_ref[...]).astype(scaled_ref.dtype)

    return _pcall(body, name=name, grid=(nb, POOL_G),
                  in_specs=[blk, pl.BlockSpec((1, LANE, LANE), lambda b, g: (g, 0, 0)),
                            pl.BlockSpec((1, LANE), lambda b, g: (0, g))],
                  out_specs=[blk, blk, blk],
                  out_shape=[_sds((T, POOL_W), MX), _sds((T, POOL_W), F32), _sds((T, POOL_W), MX)])(z, pool_grp, pool_scale)


def _pool_bwd(dscaled, mixed, pooled, pool_grp, pool_scale, *, nb, name):
    T = dscaled.shape[0]
    S = T // nb
    blk = pl.BlockSpec((S, LANE), lambda g, b: (b, g))

    def body(ds_ref, mixed_ref, pooled_ref, w_ref, s_ref, du_ref, dw_ref, dsc_ref):
        g = pl.program_id(0)
        b = pl.program_id(1)
        ds = ds_ref[...]
        dsc_ref[0] = jnp.sum(ds * mixed_ref[...], axis=0, keepdims=True)
        dmixed = (ds * s_ref[...]).astype(MX)
        dw = lax.dot_general(pooled_ref[...], dmixed, (((0,), (0,)), ((), ())), preferred_element_type=F32)

        @pl.when(b == 0)
        def _():
            dw_ref[0] = dw

        @pl.when(b > 0)
        def _():
            dw_ref[0] += dw
        dpooled = lax.dot_general(dmixed, w_ref[0], (((1,), (1,)), ((), ())), preferred_element_type=F32)
        row = lax.broadcasted_iota(jnp.int32, dpooled.shape, 0)
        width = lax.shift_left(jnp.int32(2), g)
        q = dpooled / jnp.minimum(row + 1, width).astype(F32)
        r2 = q + _shift_up(q, 1, row, S)
        r4 = r2 + _shift_up(r2, 2, row, S)
        r8 = r4 + _shift_up(r4, 4, row, S)
        r16 = r8 + _shift_up(r8, 8, row, S)
        win = jnp.where(g == 0, r2, jnp.where(g == 1, r4, jnp.where(g == 2, r8, r16)))
        du_ref[...] = (win - dpooled).astype(du_ref.dtype)

    return _pcall(body, name=name, grid=(POOL_G, nb),
                  in_specs=[blk, blk, blk, pl.BlockSpec((1, LANE, LANE), lambda g, b: (g, 0, 0)),
                            pl.BlockSpec((1, LANE), lambda g, b: (0, g))],
                  out_specs=[blk, pl.BlockSpec((1, LANE, LANE), lambda g, b: (g, 0, 0)),
                             pl.BlockSpec((1, 1, LANE), lambda g, b: (b, 0, g))],
                  out_shape=[_sds((T, POOL_W), MX), _sds((POOL_G, LANE, LANE), F32), _sds((nb, 1, POOL_W), F32)],
                  )(dscaled, mixed, pooled, pool_grp, pool_scale)


def _lane_masks(shape):
    lane = lax.broadcasted_iota(jnp.int32, shape, len(shape) - 1)
    m_n = lane < NOPE
    m_r = jnp.logical_and(lane >= KR_LANE, lane < KR_LANE + ROPE)
    first_half = lane < KR_LANE + ROPE // 2
    return m_n, m_r, first_half


def _rot(y, first_half):
    return jnp.where(first_half, -pltpu.roll(y, LANE - ROPE // 2, 1), pltpu.roll(y, ROPE // 2, 1))


def _rot_t(v, first_half, m_r):
    return jnp.where(m_r, jnp.where(first_half, pltpu.roll(v, LANE - ROPE // 2, 1), -pltpu.roll(v, ROPE // 2, 1)), 0.0)


def _mla_in_fwd(z, w_q, w_kv, qa_gain, kva_gain, cos, sin, q_gain, k_gain, *, name, tm=256):
    T = z.shape[0]
    slab = pl.BlockSpec((tm, LANE), lambda i: (i, 0))
    vec = pl.BlockSpec((1, LANE), lambda i: (0, 0))
    whole = lambda a: pl.BlockSpec(a.shape, lambda i: (0, 0))

    def body(z_ref, wq_ref, wkv_ref, qa_ref, kva_ref, cos_ref, sin_ref, qg_ref, kg_ref,
             qn_ref, kvn_ref, qp_ref, kvp_ref, q_ref, k_ref, v_ref):
        ql = z_ref[:, Z_QL:Z_QL + QL].astype(F32)
        kvl = z_ref[:, Z_KV:Z_KV + KVL].astype(F32)
        qn = (ql * _rsq(ql) * qa_ref[...]).astype(MX)
        kvn = (kvl * _rsq(kvl) * kva_ref[...]).astype(MX)
        qn_ref[...] = qn
        kvn_ref[...] = kvn
        qp = jnp.dot(qn, wq_ref[...], preferred_element_type=F32)
        kvp = jnp.dot(kvn, wkv_ref[...], preferred_element_type=F32)
        qp_ref[...] = qp
        kvp_ref[...] = kvp
        m_n, m_r, first_half = _lane_masks((tm, LANE))
        c = cos_ref[...]
        s = sin_ref[...]
        qg = qg_ref[...]
        kg = kg_ref[...]
        xr = z_ref[:, Z_KR:Z_KR + LANE].astype(F32)
        rr = lax.rsqrt(jnp.sum(xr * xr, axis=-1, keepdims=True) * (1.0 / ROPE) + EPS)
        yr = xr * rr * kg
        kr = jnp.where(m_r, yr * c + _rot(yr, first_half) * s, 0.0)
        for h in range(NH):
            x = qp[:, h * LANE:(h + 1) * LANE]
            x2 = x * x
            rn = lax.rsqrt(jnp.sum(jnp.where(m_n, x2, 0.0), axis=-1, keepdims=True) * (1.0 / NOPE) + EPS)
            rq = lax.rsqrt(jnp.sum(jnp.where(m_r, x2, 0.0), axis=-1, keepdims=True) * (1.0 / ROPE) + EPS)
            y = x * jnp.where(m_n, rn, jnp.where(m_r, rq, 0.0)) * qg
            q_ref[:, h * LANE:(h + 1) * LANE] = ((y * c + _rot(y, first_half) * s) * ATTN_SCALE).astype(q_ref.dtype)
            xk = kvp[:, h * LANE:(h + 1) * LANE]
            rk = lax.rsqrt(jnp.sum(jnp.where(m_n, xk * xk, 0.0), axis=-1, keepdims=True) * (1.0 / NOPE) + EPS)
            k_ref[:, h * LANE:(h + 1) * LANE] = (jnp.where(m_n, xk * rk * kg, 0.0) + kr).astype(k_ref.dtype)
        v_ref[...] = kvp[:, NH * LANE:].astype(v_ref.dtype)

    return _pcall(body, name=name, grid=(T // tm,),
                  in_specs=[_row_spec(tm, LAT_W), whole(w_q), whole(w_kv), whole(qa_gain), whole(kva_gain), slab, slab, vec, vec],
                  out_specs=[_row_spec(tm, QL), _row_spec(tm, KVL), _row_spec(tm, NH * LANE), _row_spec(tm, NH * LANE + NH * NOPE),
                             _row_spec(tm, NH * LANE), _row_spec(tm, NH * LANE), _row_spec(tm, NH * NOPE)],
                  out_shape=[_sds((T, QL), MX), _sds((T, KVL), MX), _sds((T, NH * LANE), F32), _sds((T, NH * LANE + NH * NOPE), F32),
                             _sds((T, NH * LANE), MX), _sds((T, NH * LANE), MX), _sds((T, NH * NOPE), MX)],
                  )(z, w_q, w_kv, qa_gain, kva_gain, cos, sin, q_gain, k_gain)


def _mla_in_bwd(dq, dk, dv, qp, kvp, z, w_q, w_kv, qa_gain, kva_gain, cos, sin, q_gain, k_gain, *, nb, name, tm=256):
    T = dq.shape[0]
    tps = (T // nb) // tm
    slab = pl.BlockSpec((tm, LANE), lambda i: (i, 0))
    vec = pl.BlockSpec((1, LANE), lambda i: (0, 0))
    whole = lambda a: pl.BlockSpec(a.shape, lambda i: (0, 0))

    def latent_bwd(x, dy, gain):
        r = _rsq(x)
        xhat = x * r
        dxhat = dy * gain
        return r * (dxhat - xhat * jnp.mean(dxhat * xhat, axis=-1, keepdims=True)), jnp.sum(dy * xhat, axis=0, keepdims=True)

    def body(dq_ref, dk_ref, dv_ref, qp_ref, kvp_ref, z_ref, wq_ref, wkv_ref, qa_ref, kva_ref, cos_ref, sin_ref, qg_ref, kg_ref,
             dqp_ref, dkvp_ref, dkr_ref, dql_ref, dkvl_ref, st_ref, sq_ref, sk_ref):
        i = pl.program_id(0)
        qp, kvp = qp_ref, kvp_ref
        m_n, m_r, first_half = _lane_masks((tm, LANE))
        c = cos_ref[...]
        s = sin_ref[...]
        qg = qg_ref[...]
        kg = kg_ref[...]
        dqg = jnp.zeros((1, LANE), F32)
        dkg = jnp.zeros((1, LANE), F32)
        dkr_sum = jnp.zeros((tm, LANE), F32)
        for h in range(NH):
            x = qp[:, h * LANE:(h + 1) * LANE]
            x2 = x * x
            rn = lax.rsqrt(jnp.sum(jnp.where(m_n, x2, 0.0), axis=-1, keepdims=True) * (1.0 / NOPE) + EPS)
            rq = lax.rsqrt(jnp.sum(jnp.where(m_r, x2, 0.0), axis=-1, keepdims=True) * (1.0 / ROPE) + EPS)
            rfac = jnp.where(m_n, rn, jnp.where(m_r, rq, 0.0))
            xhat = x * rfac
            do = dq_ref[:, h * LANE:(h + 1) * LANE] * ATTN_SCALE
            dy = do * c + _rot_t(do * s, first_half, m_r)
            dqg = dqg + jnp.sum(dy * xhat, axis=0, keepdims=True)
            dxhat = dy * qg
            t = dxhat * xhat
            mean_n = jnp.sum(jnp.where(m_n, t, 0.0), axis=-1, keepdims=True) * (1.0 / NOPE)
            mean_r = jnp.sum(jnp.where(m_r, t, 0.0), axis=-1, keepdims=True) * (1.0 / ROPE)
            dqp_ref[:, h * LANE:(h + 1) * LANE] = (
                rfac * (dxhat - xhat * jnp.where(m_n, mean_n, jnp.where(m_r, mean_r, 0.0)))).astype(dqp_ref.dtype)

            xk = kvp[:, h * LANE:(h + 1) * LANE]
            rk = lax.rsqrt(jnp.sum(jnp.where(m_n, xk * xk, 0.0), axis=-1, keepdims=True) * (1.0 / NOPE) + EPS)
            khat = jnp.where(m_n, xk * rk, 0.0)
            dko = dk_ref[:, h * LANE:(h + 1) * LANE]
            dkn = jnp.where(m_n, dko, 0.0)
            dkg = dkg + jnp.sum(dkn * khat, axis=0, keepdims=True)
            dkhat = dkn * kg
            mean_k = jnp.sum(dkhat * khat, axis=-1, keepdims=True) * (1.0 / NOPE)
            dkvp_ref[:, h * LANE:(h + 1) * LANE] = jnp.where(m_n, rk * (dkhat - khat * mean_k), 0.0).astype(dkvp_ref.dtype)
            dkr_sum = dkr_sum + jnp.where(m_r, dko, 0.0)
        dkvp_ref[:, NH * LANE:] = dv_ref[...].astype(dkvp_ref.dtype)

        xr = z_ref[:, Z_KR:Z_KR + LANE].astype(F32)
        rr = lax.rsqrt(jnp.sum(xr * xr, axis=-1, keepdims=True) * (1.0 / ROPE) + EPS)
        rhat = xr * rr
        dyr = dkr_sum * c + _rot_t(dkr_sum * s, first_half, m_r)
        dkg = dkg + jnp.sum(dyr * rhat, axis=0, keepdims=True)
        drhat = dyr * kg
        mean_kr = jnp.sum(drhat * rhat, axis=-1, keepdims=True) * (1.0 / ROPE)
        dkr_ref[...] = jnp.where(m_r, rr * (drhat - rhat * mean_kr), 0.0).astype(dkr_ref.dtype)

        nt = (((1,), (1,)), ((), ()))
        dqn = lax.dot_general(dqp_ref[...], wq_ref[...], nt, preferred_element_type=F32)
        dkvn = lax.dot_general(dkvp_ref[...], wkv_ref[...], nt, preferred_element_type=F32)
        dql, dqa = latent_bwd(z_ref[:, Z_QL:Z_QL + QL].astype(F32), dqn, qa_ref[...])
        dkvl, dkva = latent_bwd(z_ref[:, Z_KV:Z_KV + KVL].astype(F32), dkvn, kva_ref[...])
        dql_ref[...] = dql.astype(dql_ref.dtype)
        dkvl_ref[...] = dkvl.astype(dkvl_ref.dtype)

        @pl.when(i % tps == 0)
        def _():
            st_ref[...] = jnp.zeros_like(st_ref)
            sq_ref[...] = jnp.zeros_like(sq_ref)
            sk_ref[...] = jnp.zeros_like(sk_ref)

        st_ref[0, 0:1, :] += dqg
        st_ref[0, 1:2, :] += dkg
        sq_ref[0, 0:1, :] += dqa
        sk_ref[0, 0:1, :] += dkva

    stat = lambda w: pl.BlockSpec((1, SUBLANE, w), lambda i: (i // tps, 0, 0))
    return _pcall(body, name=name, grid=(T // tm,),
                  in_specs=[_row_spec(tm, NH * LANE), _row_spec(tm, NH * LANE), _row_spec(tm, NH * NOPE),
                            _row_spec(tm, NH * LANE), _row_spec(tm, NH * LANE + NH * NOPE), _row_spec(tm, LAT_W),
                            whole(w_q), whole(w_kv), whole(qa_gain), whole(kva_gain), slab, slab, vec, vec],
                  out_specs=[_row_spec(tm, NH * LANE), _row_spec(tm, NH * LANE + NH * NOPE), slab, _row_spec(tm, QL),
                             _row_spec(tm, KVL), stat(LANE), stat(QL), stat(KVL)],
                  out_shape=[_sds((T, NH * LANE), MX), _sds((T, NH * LANE + NH * NOPE), MX), _sds((T, LANE), MX),
                             _sds((T, QL), MX), _sds((T, KVL), MX), _sds((nb, SUBLANE, LANE), F32),
                             _sds((nb, SUBLANE, QL), F32), _sds((nb, SUBLANE, KVL), F32)],
                  )(dq, dk, dv, qp, kvp, z, w_q, w_kv, qa_gain, kva_gain, cos, sin, q_gain, k_gain)


def _lower_triangle(t, keys_first=False):
    key_axis = 0 if keys_first else 1
    return lax.broadcasted_iota(jnp.int32, (t, t), key_axis) <= lax.broadcasted_iota(jnp.int32, (t, t), 1 - key_axis)


def _attn_fwd(q, k, v, *, nb, name, tq=512, comm=None):
    T = q.shape[0]
    S = T // nb
    tq = _pick(S, tq)
    nq = S // tq
    tk = tq
    npair = NH // 2

    def body(q_ref, k_ref, v_ref, o_ref, lse_ref):
        qi = pl.program_id(2)
        lane = lax.broadcasted_iota(jnp.int32, (tq, LANE), 1)
        qs = [q_ref[:, hh * LANE:(hh + 1) * LANE] for hh in range(2)]

        def block(j, carry, diagonal):
            k0 = pl.multiple_of(j * tk, tk)
            vb = v_ref[pl.ds(k0, tk), :]
            vbs = [jnp.where(lane < NOPE, vb, jnp.ones_like(vb)), jnp.where(lane < NOPE, jnp.ones_like(vb), vb)]
            new = []
            for hh in range(2):
                m, acc = carry[hh]
                kb = k_ref[pl.ds(k0, tk), hh * LANE:(hh + 1) * LANE]
                s = lax.dot_general(qs[hh], kb, (((1,), (1,)), ((), ())), preferred_element_type=F32)
                if diagonal:
                    s = jnp.where(_lower_triangle(tq), s, NEG)
                m_new = jnp.maximum(m, jnp.max(s, axis=-1, keepdims=True))
                p = jnp.exp((s - m_new).astype(MX))
                acc = jnp.exp(m - m_new) * acc + jnp.dot(p, vbs[hh], preferred_element_type=F32)
                new.append((m_new, acc))
            return tuple(new)

        init = tuple((jnp.full((tq, 1), NEG, F32), jnp.zeros((tq, LANE), F32)) for _ in range(2))
        carry = lax.fori_loop(0, qi, lambda j, c: block(j, c, False), init)
        (m0, acc0), (m1, acc1) = block(qi, carry, True)
        l0, l1 = acc0[:, NOPE:NOPE + 1], acc1[:, 0:1]
        o_ref[...] = jnp.where(lane < NOPE, acc0 / l0, acc1 / l1).astype(o_ref.dtype)
        lse_ref[...] = jnp.where(lane < NOPE, m0 + jnp.log(l0), m1 + jnp.log(l1))

    return _pcall(body, name=name, grid=(nb, npair, nq),
                  in_specs=[pl.BlockSpec((tq, 2 * LANE), lambda b, p, i: (b * nq + i, p)),
                            pl.BlockSpec((S, 2 * LANE), lambda b, p, i: (b, p)),
                            pl.BlockSpec((S, LANE), lambda b, p, i: (b, p))],
                  out_specs=[pl.BlockSpec((tq, LANE), lambda b, p, i: (b * nq + i, p)),
                             pl.BlockSpec((tq, LANE), lambda b, p, i: (b * nq + i, p))],
                  out_shape=[_sds((T, NH * NOPE), MX), _sds((T, NH * NOPE), F32)], comm=comm)(q, k, v)


def _attn_bwd(q, k, v, o, lse, do, *, nb, name, tq=512, comm=None):
    T = q.shape[0]
    S = T // nb
    tq = _pick(S, tq)
    nq = S // tq
    tk = tq
    npair = NH // 2

    def body(q_ref, k_ref, v_ref, o_ref, lse_ref, do_ref, dq_ref, dk_ref, dv_ref, lse_t, delta_t):
        lane = lax.broadcasted_iota(jnp.int32, (tq, LANE), 1)
        first = lane < NOPE
        dq_ref[...] = jnp.zeros_like(dq_ref)

        def rows_step(qi, _):
            q0 = pl.multiple_of(qi * tq, tq)
            prod = do_ref[pl.ds(q0, tq), :] * o_ref[pl.ds(q0, tq), :].astype(F32)
            d0 = jnp.sum(jnp.where(first, prod, 0.0), axis=-1, keepdims=True)
            d1 = jnp.sum(jnp.where(first, 0.0, prod), axis=-1, keepdims=True)
            delta_t[:, pl.ds(q0, tq)] = jnp.where(first, d0, d1).T
            lse_t[:, pl.ds(q0, tq)] = lse_ref[pl.ds(q0, tq), :].T
            return 0

        lax.fori_loop(0, nq, rows_step, 0)

        def kv_step(kj, _):
            k0 = pl.multiple_of(kj * tk, tk)
            kbs = [k_ref[pl.ds(k0, tk), hh * LANE:(hh + 1) * LANE] for hh in range(2)]
            vb = v_ref[pl.ds(k0, tk), :]

            def q_block(qi, carry, diagonal):
                dk0, dk1, dv = carry
                dks = [dk0, dk1]
                q0 = pl.multiple_of(qi * tq, tq)
                dov = do_ref[pl.ds(q0, tq), :]
                for hh in range(2):
                    qh = q_ref[pl.ds(q0, tq), hh * LANE:(hh + 1) * LANE]
                    dob = jnp.where(first if hh == 0 else jnp.logical_not(first), dov, 0.0).astype(MX)
                    st = lax.dot_general(kbs[hh], qh, (((1,), (1,)), ((), ())), preferred_element_type=F32)
                    pt = jnp.exp((st - lse_t[hh * NOPE:hh * NOPE + 1, pl.ds(q0, tq)]).astype(MX))
                    if diagonal:
                        pt = jnp.where(_lower_triangle(tq, keys_first=True), pt, jnp.zeros_like(pt))
                    dpt = lax.dot_general(vb, dob, (((1,), (1,)), ((), ())), preferred_element_type=F32)
                    dst = pt * (dpt - delta_t[hh * NOPE:hh * NOPE + 1, pl.ds(q0, tq)]).astype(MX)
                    dks[hh] = dks[hh] + jnp.dot(dst, qh, preferred_element_type=F32)
                    dv = dv + jnp.dot(pt, dob, preferred_element_type=F32)
                    dq_ref[pl.ds(q0, tq), hh * LANE:(hh + 1) * LANE] += lax.dot_general(
                        dst, kbs[hh], (((0,), (0,)), ((), ())), preferred_element_type=F32)
                return dks[0], dks[1], dv

            zero = jnp.zeros((tk, LANE), F32)
            carry = q_block(kj, (zero, zero, zero), True)
            dk0, dk1, dv = lax.fori_loop(kj + 1, nq, lambda qi, c: q_block(qi, c, False), carry)
            dk_ref[pl.ds(k0, tk), 0:LANE] = dk0
            dk_ref[pl.ds(k0, tk), LANE:2 * LANE] = dk1
            dv_ref[pl.ds(k0, tk), :] = dv
            return 0

        lax.fori_loop(0, nq, kv_step, 0)

    pair256 = pl.BlockSpec((S, 2 * LANE), lambda b, p: (b, p))
    pair128 = pl.BlockSpec((S, LANE), lambda b, p: (b, p))
    return _pcall(body, name=name, grid=(nb, npair),
                  in_specs=[pair256, pair256, pair128, pair128, pair128, pair128],
                  out_specs=[pair256, pair256, pair128],
                  out_shape=[_sds((T, NH * LANE), F32), _sds((T, NH * LANE), F32), _sds((T, NH * NOPE), F32)],
                  scratch=[pltpu.VMEM((LANE, S), F32), pltpu.VMEM((LANE, S), F32)], comm=comm)(q, k, v, o, lse, do)


def _run(plan, fn, *args, name, **kw):
    rider = plan.rider(name)
    if rider is None:
        return fn(*args, name=name, **kw)
    outs, landed = fn(*args, name=name, comm=rider, **kw)
    plan.landed(name, landed)
    return outs


def _layer_fwd_bwd(x, tgt, mod3, cos, sin, plan, P):
    nb = mod3.shape[0]
    W = plan.W
    h1, gu1, a1 = _run(plan, _ffn_up_first, x, mod3, P["norm_ffn1"], W["ffn1_in"], sub=0, name="ffn1_up_a")
    gu1, a1 = _run(plan, _ffn_up_second, h1, W["ffn1_in"], gu1, a1, name="ffn1_up_b")
    f1, x1, h2 = _run(plan, _mm_resid_norm, a1, W["ffn1_out"], x, mod3, P["norm_mix"], sub=1, coef=0.5, name="ffn1_out")
    z = _run(plan, _mm, h2, W["w_in"], mode="nt", out_dtype=MX, name="mix_in", tn=1664)
    pooled, mixed, scaled = _pool_fwd(z, W["pool_grp"], P["pool_scale"], nb=nb, name="pool_fwd")
    br_pool = _mm(scaled, W["pool_proj"], mode="nn", out_dtype=MX, name="pool_proj")
    qn, kvn, qp, kvp, q, k, v = _mla_in_fwd(z, W["q_up"], W["kv_up"], P["q_a_norm"], P["kv_a_norm"], cos, sin,
                                            P["q_gain"], P["k_gain"], name="mla_in_fwd")
    attn, lse = _run(plan, _attn_fwd, q, k, v, nb=nb, name="attn_fwd")
    br_mla, merged, mo, x2, h3 = _mix_out_fwd(z, br_pool, attn, W["mla_proj"], W["w_out"], x1, mod3, P["norm_ffn2"], sub=2,
                                              name="mix_out", tm=256)
    gu2, a2 = _ffn_up(h3, W["ffn2_in"], name="ffn2_up")
    dy, df2, st_fin, loss = _mm_loss(a2, W["ffn2_out"], x2, tgt, mod3, name="ffn2_out_loss")

    plan.grad("ffn2_out", _dw(a2, df2, name="d_ffn2_out"))
    dgu2 = _ffn_dact(df2, gu2, W["ffn2_out"], name="ffn2_dact")
    plan.grad("ffn2_in", _ffn_dw_in(h3, dgu2, name="d_ffn2_in"))
    dx2, dmo, st3 = _run(plan, _ffn_dh_norm, dgu2, W["ffn2_in"], x2, dy, mo, mod3, P["norm_ffn2"], sub=2, coef=1.0,
                         name="d_ffn2_h")
    plan.grad("w_out", _dw(merged, dmo, name="d_mix_out"))
    dbr_pool, dbr_mla, dgates, dattn = _mix_out_bwd(dmo, W["w_out"], z, br_pool, br_mla, W["mla_proj"], name="mix_out_bwd", tm=256)
    plan.grad("pool_proj", _dw(scaled, dbr_pool, name="d_pool_proj"))
    dscaled = _mm(dbr_pool, W["pool_proj"], mode="nt", out_dtype=F32, name="d_pool_scaled")
    du_pool, d_pool_grp, d_pool_scale = _pool_bwd(dscaled, mixed, pooled, W["pool_grp"], P["pool_scale"], nb=nb, name="pool_bwd")
    plan.grad("mla_proj", _dw(attn, dbr_mla, name="d_mla_proj"))
    dq, dk, dv = _run(plan, _attn_bwd, q, k, v, attn, lse, dattn, nb=nb, name="attn_bwd")
    dqp, dkvp, dkr, dql, dkvl, st_prep, st_q, st_kv = _mla_in_bwd(
        dq, dk, dv, qp, kvp, z, W["q_up"], W["kv_up"], P["q_a_norm"], P["kv_a_norm"], cos, sin, P["q_gain"], P["k_gain"],
        nb=nb, name="mla_in_bwd")
    plan.grad("q_up", _dw(qn, dqp, name="d_q_up"))
    plan.grad("kv_up", _dw(kvn, dkvp, name="d_kv_up"))
    dz = jnp.concatenate([du_pool, dql, dkvl, dkr, dgates], axis=1)
    plan.grad("w_in", _run(plan, _dw, h2, dz, name="d_mix_in", tm=256, tn=1664, out_t=True))
    dx1, df1, st2 = _run(plan, _mm_norm_bwd, dz, W["w_in"], x1, dx2, f1, mod3, P["norm_mix"], sub=1, coef=0.5, name="d_mix_h")
    plan.grad("ffn1_out", _run(plan, _dw, a1, df1, name="d_ffn1_out"))
    dgu1 = _run(plan, _ffn_dact, df1, gu1, W["ffn1_out"], name="ffn1_dact")
    half = x.shape[1] // 2
    plan.grad("ffn1_in@0", _run(plan, _ffn_dw_in, h1, dgu1, rows=(0, half), name="d_ffn1_in_a"))
    plan.grad("ffn1_in@1", _run(plan, _ffn_dw_in, h1, dgu1, rows=(half, half), name="d_ffn1_in_b"))
    dh1 = _run(plan, _ffn_dh, dgu1, W["ffn1_in"], name="d_ffn1_h")
    grad_x, st1 = _run(plan, _bwd_block, x, dh1, dx1, None, mod3, P["norm_ffn1"], sub=0, coef=0.0, name="bwd_norm1")

    return loss, grad_x, (st1, st2, st3, st_fin, st_q, st_kv, st_prep, d_pool_scale), d_pool_grp


def _padded(rows):
    return rows + -rows % (4 * SUBLANE)


def _pad_rows(a):
    pad = [(0, 0)] * a.ndim
    pad[-2] = (0, _padded(a.shape[-2]) - a.shape[-2])
    return jnp.pad(a, pad)


def _w_in_pieces(shard):
    runs = []
    for ref0, ref1, k0 in ((0, Z_KR, 0), (Z_KR, Z_KR + ROPE, Z_KR + KR_LANE), (Z_KR + ROPE, N_CHIP * shard, Z_GP)):
        for q in range(N_CHIP):
            lo, hi = max(ref0, q * shard), min(ref1, (q + 1) * shard)
            if lo < hi:
                runs.append((k0 + lo - ref0, q * _padded(shard) + lo - q * shard, hi - lo))
    return runs


def _w_in_stacked_to_kernel(st, shard):
    flat = st.reshape(-1, st.shape[2])
    parts, at = [], 0
    for k0, s0, n in sorted(_w_in_pieces(shard)):
        parts += [jnp.zeros((k0 - at, flat.shape[1]), st.dtype)] * (k0 > at) + [flat[s0:s0 + n]]
        at = k0 + n
    assert at == Z_W
    return jnp.concatenate(parts, axis=0)


def _w_in_kernel_to_stacked(gt, shard):
    parts, at = [], 0
    for s0, k0, n in sorted((s, k, n) for k, s, n in _w_in_pieces(shard)) + [(N_CHIP * _padded(shard), 0, 0)]:
        parts += [jnp.zeros((s0 - at, gt.shape[1]), gt.dtype)] * (s0 > at) + [gt[k0:k0 + n]] * (n > 0)
        at = s0 + n
    return jnp.concatenate(parts, axis=0).reshape(N_CHIP, _padded(shard), gt.shape[1])


def _q_up_to_kernel(w):
    k = w.shape[0]
    return jnp.pad(w.reshape(k, NH, NOPE + ROPE), ((0, 0), (0, 0), (0, LANE - NOPE - ROPE))).reshape(k, NH * LANE)


def _q_up_from_kernel(g):
    k = g.shape[0]
    return g.reshape(k, NH, LANE)[:, :, :NOPE + ROPE].reshape(k, NH * (NOPE + ROPE))


def _kv_up_to_kernel(w):
    k = w.shape[0]
    w3 = w.reshape(k, NH, 2 * NOPE)
    kpart = jnp.pad(w3[:, :, :NOPE], ((0, 0), (0, 0), (0, LANE - NOPE))).reshape(k, NH * LANE)
    return jnp.concatenate([kpart, w3[:, :, NOPE:].reshape(k, NH * NOPE)], axis=1)


def _kv_up_from_kernel(g):
    k = g.shape[0]
    kpart = g[:, :NH * LANE].reshape(k, NH, LANE)[:, :, :NOPE]
    vpart = g[:, NH * LANE:].reshape(k, NH, NOPE)
    return jnp.concatenate([kpart, vpart], axis=2).reshape(k, NH * 2 * NOPE)


def _gain_slab(nope, rope):
    return jnp.concatenate([nope, rope, jnp.zeros((1, LANE - NOPE - ROPE), nope.dtype)], axis=1)


def _rope_tables(positions):
    inv_freq = 10000.0 ** (-jnp.arange(0, ROPE, 2, dtype=F32) / ROPE)
    ang = positions.astype(F32)[:, None] * inv_freq
    ang = jnp.concatenate([ang, ang], axis=-1)
    t = positions.shape[0]
    cos = jnp.concatenate([jnp.ones((t, KR_LANE), F32), jnp.cos(ang), jnp.ones((t, LANE - KR_LANE - ROPE), F32)], axis=1)
    sin = jnp.concatenate([jnp.zeros((t, KR_LANE), F32), jnp.sin(ang), jnp.zeros((t, LANE - KR_LANE - ROPE), F32)], axis=1)
    return cos, sin


def _coords():
    return lax.axis_index("x"), lax.axis_index("y"), lax.axis_index("c")


HBM_SPEC = pl.BlockSpec(memory_space=pl.ANY)
VMEM_SPEC = pl.BlockSpec(memory_space=pltpu.VMEM)


CHIP_RELS = ((1, 0), (0, 1), (1, 1))


def _gather_comm(shards):
    n = len(shards)

    def ici(ins, outs, sems, a, j, x, y, cc):
        half = ins[a].shape[0] // 2
        mine = pl.ds(cc * half, half)
        dx, dy = CHIP_RELS[j]
        return pltpu.make_async_remote_copy(src_ref=ins[a].at[mine], dst_ref=outs[a].at[2 * x + y, mine],
                                            send_sem=sems[0].at[a, j], recv_sem=sems[1].at[a, j],
                                            device_id=(x ^ dx, y ^ dy, cc), device_id_type=MESH)

    def d2d(ins, outs, sems, a, j, x, y, cc, half_of):
        half = ins[a].shape[0] // 2
        dx, dy = CHIP_RELS[j]
        landed = outs[a].at[2 * (x ^ dx) + (y ^ dy), pl.ds(half_of * half, half)]
        return pltpu.make_async_remote_copy(src_ref=landed, dst_ref=landed, send_sem=sems[2].at[a, j], recv_sem=sems[3].at[a, j],
                                            device_id=(x, y, 1 - cc), device_id_type=MESH)

    def own(ins, outs, sems, a, x, y):
        return pltpu.make_async_copy(ins[a], outs[a].at[2 * x + y], sems[4].at[a])

    def start(ins, outs, sems):
        x, y, cc = _coords()
        for a in range(n):
            own(ins, outs, sems, a, x, y).start()
            for j in range(3):
                ici(ins, outs, sems, a, j, x, y, cc).start()

    def finish(ins, outs, sems):
        x, y, cc = _coords()
        for a in range(n):
            for j in range(3):
                ici(ins, outs, sems, a, j, x, y, cc).wait_recv()
                d2d(ins, outs, sems, a, j, x, y, cc, cc).start()
        for a in range(n):
            for j in range(3):
                d2d(ins, outs, sems, a, j, x, y, cc, 1 - cc).wait_recv()
        for a in range(n):
            for j in range(3):
                ici(ins, outs, sems, a, j, x, y, cc).wait_send()
                d2d(ins, outs, sems, a, j, x, y, cc, cc).wait_send()
            own(ins, outs, sems, a, x, y).wait()

    dma = pltpu.SemaphoreType.DMA
    return _Comm(shards, [_sds((N_CHIP,) + s.shape, s.dtype) for s in shards],
                 [dma((n, 3)), dma((n, 3)), dma((n, 3)), dma((n, 3)), dma((n,))], start, finish)


def _swap_comm(parts):
    n = len(parts)

    def copy(ins, outs, sems, a):
        x, y, cc = _coords()
        half = ins[a].shape[1] // 2
        return pltpu.make_async_remote_copy(src_ref=ins[a].at[:, pl.ds((1 - cc) * half, half)], dst_ref=outs[a],
                                            send_sem=sems[0].at[a], recv_sem=sems[1].at[a], device_id=(x, y, 1 - cc),
                                            device_id_type=MESH)

    def start(ins, outs, sems):
        for a in range(n):
            copy(ins, outs, sems, a).start()

    def finish(ins, outs, sems):
        for a in range(n):
            copy(ins, outs, sems, a).wait()

    dma = pltpu.SemaphoreType.DMA
    return _Comm(parts, [_sds((p.shape[0], p.shape[1] // 2, p.shape[2]), p.dtype) for p in parts], [dma((n,)), dma((n,))],
                 start, finish)


def _add_half(full, other, cidx, *, name):
    nch, r, c = full.shape
    half = r // 2
    tr = _pick_rows(half)
    nbk = half // tr
    grid_spec = pltpu.PrefetchScalarGridSpec(
        num_scalar_prefetch=1, grid=(nch, nbk),
        in_specs=[pl.BlockSpec((1, tr, c), lambda j, i, cref: (j, cref[0] * nbk + i, 0)),
                  pl.BlockSpec((1, tr, c), lambda j, i, cref: (j, i, 0))],
        out_specs=pl.BlockSpec((1, tr, c), lambda j, i, cref: (j, i, 0)))

    def body(cref, a_ref, b_ref, o_ref):
        o_ref[...] = (a_ref[...] + b_ref[...]).astype(o_ref.dtype)

    return _pcall(body, name=name, out_shape=_sds((nch, half, c), MX), grid_spec=grid_spec)(cidx, full, other)


def _pick_rows(rows, target=512):
    best = None
    for t in range(16, min(rows, target) + 1, 16):
        if rows % t == 0:
            best = t
    return rows if best is None else best


def _exchange_comm(parts):
    n = len(parts)

    def send(ins, outs, sems, a, j, x, y, cc):
        dx, dy = CHIP_RELS[j]
        return pltpu.make_async_remote_copy(src_ref=ins[a].at[2 * (x ^ dx) + (y ^ dy)], dst_ref=outs[a].at[2 * x + y],
                                            send_sem=sems[0].at[a, j], recv_sem=sems[1].at[a, j],
                                            device_id=(x ^ dx, y ^ dy, cc), device_id_type=MESH)

    def landing(ins, outs, sems, a, j, x, y, cc):
        dx, dy = CHIP_RELS[j]
        peer_chip = 2 * (x ^ dx) + (y ^ dy)
        return pltpu.make_async_remote_copy(src_ref=ins[a].at[peer_chip], dst_ref=outs[a].at[peer_chip], send_sem=sems[0].at[a, j],
                                            recv_sem=sems[1].at[a, j], device_id=(x, y, cc), device_id_type=MESH)

    def own(ins, outs, sems, a, x, y):
        return pltpu.make_async_copy(ins[a].at[2 * x + y], outs[a].at[2 * x + y], sems[2].at[a])

    def start(ins, outs, sems):
        x, y, cc = _coords()
        for a in range(n):
            own(ins, outs, sems, a, x, y).start()
            for j in range(3):
                send(ins, outs, sems, a, j, x, y, cc).start()

    def finish(ins, outs, sems):
        x, y, cc = _coords()
        for a in range(n):
            for j in range(3):
                landing(ins, outs, sems, a, j, x, y, cc).wait_recv()
        for a in range(n):
            for j in range(3):
                send(ins, outs, sems, a, j, x, y, cc).wait_send()
            own(ins, outs, sems, a, x, y).wait()

    dma = pltpu.SemaphoreType.DMA
    return _Comm(parts, [_sds(p.shape, p.dtype) for p in parts], [dma((n, 3)), dma((n, 3)), dma((n,))], start, finish)


def _sum_chips(q, cidx, *, name):
    nch, h, c = q.shape
    tr = _pick_rows(h)
    nbk = h // tr
    grid_spec = pltpu.PrefetchScalarGridSpec(
        num_scalar_prefetch=1, grid=(nbk,),
        in_specs=[pl.BlockSpec((nch, tr, c), lambda i, cref: (0, i, 0))],
        out_specs=pl.BlockSpec((tr, c), lambda i, cref: (cref[0] * nbk + i, 0)))

    def body(cref, q_ref, o_ref):
        acc = q_ref[0].astype(F32) + q_ref[1].astype(F32)
        acc = acc + q_ref[2].astype(F32)
        o_ref[...] = acc + q_ref[3].astype(F32)

    return _pcall(body, name=name, out_shape=_sds((2 * h, c), F32), grid_spec=grid_spec)(cidx, q)


def _join_comm(fulls):
    n = len(fulls)

    def half_copy(outs, sems, a, which):
        x, y, cc = _coords()
        h = outs[a].shape[0] // 2
        rows = outs[a].at[pl.ds((cc if which == 0 else 1 - cc) * h, h)]
        return pltpu.make_async_remote_copy(src_ref=rows, dst_ref=rows, send_sem=sems[0].at[a], recv_sem=sems[1].at[a],
                                            device_id=(x, y, 1 - cc), device_id_type=MESH)

    def start(ins, outs, sems):
        for a in range(n):
            half_copy(outs, sems, a, 0).start()

    def finish(ins, outs, sems):
        for a in range(n):
            half_copy(outs, sems, a, 1).wait_recv()
        for a in range(n):
            half_copy(outs, sems, a, 0).wait_send()

    dma = pltpu.SemaphoreType.DMA
    return _Comm(fulls, [_sds(p.shape, p.dtype) for p in fulls], [dma((n,)), dma((n,))], start, finish,
                 aliases={a: a for a in range(n)})


def _ada_prologue(c, w, b, to_cast, *, name, comm=None):
    nb, dm = c.shape
    n = w.shape[1]

    nc = len(to_cast)

    def body(c_ref, w_ref, b_ref, *rest):
        src, (call_ref, land_ref), dst = rest[:nc], rest[nc:nc + 2], rest[nc + 2:2 * nc + 2]
        cpad, mod_s, g_send, g_recv, m_send, m_recv = rest[2 * nc + 2:2 * nc + 8]
        wide, narrow, in_sem, out_sem = rest[2 * nc + 8:3 * nc + 8], rest[3 * nc + 8:4 * nc + 8], rest[-2], rest[-1]
        loads = [pltpu.make_async_copy(src[i], wide[i], in_sem.at[i]) for i in range(nc)]
        stores = [pltpu.make_async_copy(narrow[i], dst[i], out_sem.at[i]) for i in range(nc)]
        for ld in loads:
            ld.start()
        x, y, cc = _coords()
        me = 4 * x + 2 * y + cc
        chip = 2 * x + y
        cpad[...] = jnp.zeros_like(cpad)
        cpad[0:nb, :] = c_ref[...]
        copies = []
        for kk in range(1, N_DEV):
            peer = (x ^ (kk >> 2), y ^ ((kk >> 1) & 1), cc ^ (kk & 1))
            cp = pltpu.make_async_remote_copy(src_ref=cpad, dst_ref=call_ref.at[me], send_sem=g_send.at[kk - 1],
                                              recv_sem=g_recv.at[kk - 1], device_id=peer, device_id_type=MESH)
            cp.start()
            copies.append(cp)
        call_ref[me] = cpad[...]
        for i in range(nc):
            loads[i].wait()
            narrow[i][...] = wide[i][...].astype(MX)
            stores[i].start()
        for kk in range(1, N_DEV):
            pltpu.make_async_remote_copy(src_ref=cpad, dst_ref=call_ref.at[me ^ kk], send_sem=g_send.at[kk - 1],
                                         recv_sem=g_recv.at[kk - 1], device_id=(x, y, cc), device_id_type=MESH).wait_recv()
        cv = call_ref[...].reshape(N_DEV * SUBLANE, dm)
        act = (cv * _sigmoid(cv)).astype(MX)
        mod = jnp.dot(act, w_ref[...].astype(MX), preferred_element_type=F32) + b_ref[...]
        mod_s[...] = mod.reshape(N_DEV, SUBLANE, n)
        for j, (dx, dy) in enumerate(CHIP_RELS):
            cp = pltpu.make_async_remote_copy(src_ref=mod_s.at[4 * (x ^ dx) + 2 * (y ^ dy) + cc], dst_ref=land_ref.at[chip],
                                              send_sem=m_send.at[j], recv_sem=m_recv.at[j],
                                              device_id=(x ^ dx, y ^ dy, cc), device_id_type=MESH)
            cp.start()
            copies.append(cp)
        land_ref[chip] = mod_s[me]
        for j, (dx, dy) in enumerate(CHIP_RELS):
            pltpu.make_async_remote_copy(src_ref=mod_s.at[me], dst_ref=land_ref.at[2 * (x ^ dx) + (y ^ dy)], send_sem=m_send.at[j],
                                         recv_sem=m_recv.at[j], device_id=(x, y, cc), device_id_type=MESH).wait_recv()
        for cp in copies:
            cp.wait_send()
        for st in stores:
            st.wait()

    dma = pltpu.SemaphoreType.DMA
    return _pcall(body, name=name, in_specs=[VMEM_SPEC] * 3 + [HBM_SPEC] * nc, out_specs=[VMEM_SPEC] * 2 + [HBM_SPEC] * nc,
                  out_shape=[_sds((N_DEV, SUBLANE, dm), F32), _sds((N_CHIP, SUBLANE, n), F32)]
                  + [_sds(a.shape, MX) for a in to_cast],
                  scratch=[pltpu.VMEM((SUBLANE, dm), F32), pltpu.VMEM((N_DEV, SUBLANE, n), F32),
                           dma((N_DEV - 1,)), dma((N_DEV - 1,)), dma((3,)), dma((3,))]
                  + [pltpu.VMEM(a.shape, a.dtype) for a in to_cast] + [pltpu.VMEM(a.shape, MX) for a in to_cast]
                  + [dma((nc,)), dma((nc,))], comm=comm)(c, w, b, *to_cast)


def _ada_bwd(c_all, dmod_cols, *, name):
    m, kdim = c_all.shape
    n = dmod_cols.shape[1]
    tn = _pick(n, 1152)

    def body(c_ref, d_ref, o_ref):
        cv = c_ref[...]
        act = (cv * _sigmoid(cv)).astype(MX)
        o_ref[...] = lax.dot_general(act, d_ref[...].astype(MX), (((0,), (0,)), ((), ())), preferred_element_type=F32)

    return _pcall(body, name=name, out_shape=_sds((kdim, n), F32), grid=(n // tn,),
                  in_specs=[pl.BlockSpec((m, kdim), lambda j: (0, 0)), pl.BlockSpec((m, tn), lambda j: (0, j))],
                  out_specs=pl.BlockSpec((kdim, tn), lambda j: (0, j)))(c_all, dmod_cols)


SLAB_W = 1024
RED_ROWS = 8


LOSS_LANE = SLAB_W - LANE


def _pack_stats(st1, st2, st3, st_fin, st_q, st_kv, st_prep, d_pool_scale, loss8, *, name):
    nb = st1.shape[0]
    dm_rows = -(-9 * nb // SUBLANE) * SUBLANE

    def body(s1, s2, s3, sf, sq, skv, sp, sps, loss_ref, o_ref):
        o_ref[...] = jnp.zeros_like(o_ref)
        for s in range(nb):
            rows = [s1[s, 0:1, :], s1[s, 1:2, :], s2[s, 3:4, :], s2[s, 0:1, :], s2[s, 1:2, :], s3[s, 3:4, :], s3[s, 0:1, :],
                    s3[s, 1:2, :], sf[s, 0:1, :]]
            for k, row in enumerate(rows):
                o_ref[9 * s + k:9 * s + k + 1, :] = row

        def over_seq(ref, r):
            acc = ref[0, r:r + 1, :]
            for s in range(1, nb):
                acc = acc + ref[s, r:r + 1, :]
            return acc

        o_ref[dm_rows + 0:dm_rows + 1, :] = over_seq(s1, 2)
        o_ref[dm_rows + 1:dm_rows + 2, :] = over_seq(s2, 2)
        o_ref[dm_rows + 2:dm_rows + 3, :] = over_seq(s3, 2)
        o_ref[dm_rows + 3:dm_rows + 4, 0:POOL_W] = over_seq(sps, 0)
        o_ref[dm_rows + 4:dm_rows + 5, 0:QL] = over_seq(sq, 0)
        o_ref[dm_rows + 5:dm_rows + 6, 0:KVL] = over_seq(skv, 0)
        o_ref[dm_rows + 6:dm_rows + 7, 0:LANE] = over_seq(sp, 0)
        o_ref[dm_rows + 7:dm_rows + 8, 0:LANE] = over_seq(sp, 1)
        o_ref[dm_rows + 7:dm_rows + 8, LOSS_LANE:] = loss_ref[0:1, :]

    return _pcall(body, name=name, out_shape=_sds((dm_rows + RED_ROWS, SLAB_W), F32), in_specs=[VMEM_SPEC] * 9,
                  out_specs=VMEM_SPEC)(st1, st2, st3, st_fin, st_q, st_kv, st_prep, d_pool_scale, loss8)


def _small_allreduce(slab, pool, *, name, comm=None):
    rows, w = slab.shape
    dm = rows - RED_ROWS
    prow, pw = pool.shape
    crow = RED_ROWS + 2 * dm

    def body(slab_ref, pool_ref, red_ref, dmod_ref, ptot_ref, sib_slab, sib_pool, chip_slab, chip_pool, land_slab, land_pool,
             a_send, a_recv, b_send, b_recv):
        x, y, cc = _coords()
        chip = 2 * x + y
        sib = (x, y, 1 - cc)
        to_sib = [pltpu.make_async_remote_copy(src_ref=slab_ref, dst_ref=sib_slab, send_sem=a_send.at[0], recv_sem=a_recv.at[0],
                                               device_id=sib, device_id_type=MESH),
                  pltpu.make_async_remote_copy(src_ref=pool_ref, dst_ref=sib_pool, send_sem=a_send.at[1], recv_sem=a_recv.at[1],
                                               device_id=sib, device_id_type=MESH)]
        for cp in to_sib:
            cp.start()
        for cp in to_sib:
            cp.wait()
        mine_dm, theirs_dm = slab_ref[0:dm, :], sib_slab[0:dm, :]
        chip_slab[0:RED_ROWS, :] = slab_ref[dm:, :] + sib_slab[dm:, :]
        chip_slab[RED_ROWS:RED_ROWS + dm, :] = jnp.where(cc == 0, mine_dm, theirs_dm)
        chip_slab[RED_ROWS + dm:, :] = jnp.where(cc == 0, theirs_dm, mine_dm)
        chip_pool[...] = pool_ref[...] + sib_pool[...]

        sends = []
        for j, (dx, dy) in enumerate(CHIP_RELS):
            peer = (x ^ dx, y ^ dy, cc)
            sends.append(pltpu.make_async_remote_copy(src_ref=chip_slab, dst_ref=land_slab.at[chip], send_sem=b_send.at[j, 0],
                                                      recv_sem=b_recv.at[j, 0], device_id=peer, device_id_type=MESH))
            sends.append(pltpu.make_async_remote_copy(src_ref=chip_pool, dst_ref=land_pool.at[chip], send_sem=b_send.at[j, 1],
                                                      recv_sem=b_recv.at[j, 1], device_id=peer, device_id_type=MESH))
        for cp in sends:
            cp.start()
        land_slab[chip] = chip_slab[...]
        land_pool[chip] = chip_pool[...]
        for j, (dx, dy) in enumerate(CHIP_RELS):
            peer_chip = 2 * (x ^ dx) + (y ^ dy)
            pltpu.make_async_remote_copy(src_ref=chip_slab, dst_ref=land_slab.at[peer_chip], send_sem=b_send.at[j, 0],
                                         recv_sem=b_recv.at[j, 0], device_id=(x, y, cc), device_id_type=MESH).wait_recv()
            pltpu.make_async_remote_copy(src_ref=chip_pool, dst_ref=land_pool.at[peer_chip], send_sem=b_send.at[j, 1],
                                         recv_sem=b_recv.at[j, 1], device_id=(x, y, cc), device_id_type=MESH).wait_recv()
        red = land_slab[0, 0:RED_ROWS, :]
        ptot = land_pool[0]
        for ch in range(1, N_CHIP):
            red = red + land_slab[ch, 0:RED_ROWS, :]
            ptot = ptot + land_pool[ch]
        red_ref[...] = red
        ptot_ref[...] = ptot
        for ch in range(N_CHIP):
            dmod_ref[2 * dm * ch:2 * dm * (ch + 1), :] = land_slab[ch, RED_ROWS:, :]
        for cp in sends:
            cp.wait_send()

    dma = pltpu.SemaphoreType.DMA
    return _pcall(body, name=name, in_specs=[VMEM_SPEC, VMEM_SPEC], out_specs=[VMEM_SPEC] * 3,
                  out_shape=[_sds((RED_ROWS, w), F32), _sds((N_DEV * dm, w), F32), _sds((prow, pw), F32)],
                  scratch=[pltpu.VMEM((rows, w), F32), pltpu.VMEM((prow, pw), F32), pltpu.VMEM((crow, w), F32),
                           pltpu.VMEM((prow, pw), F32), pltpu.VMEM((N_CHIP, crow, w), F32), pltpu.VMEM((N_CHIP, prow, pw), F32),
                           dma((2,)), dma((2,)), dma((3, 2)), dma((3, 2))], comm=comm)(slab, pool)


def _adamw_math(w, g, m, v):
    mn = ADAM_B1 * m + (1.0 - ADAM_B1) * g
    vn = ADAM_B2 * v + (1.0 - ADAM_B2) * (g * g)
    bc1 = 1.0 / (1.0 - ADAM_B1 ** ADAM_STEP)
    bc2 = 1.0 / (1.0 - ADAM_B2 ** ADAM_STEP)
    return -ADAM_LR * ((mn * bc1) / (jnp.sqrt(vn * bc2) + ADAM_EPS) + ADAM_WD * w), mn, vn


def _small_update(red, dmod_all, pool_total, nb, params, *, name):
    names = list(SMALL)
    dm = dmod_all.shape[0] // N_DEV

    def grad_of(nm, red_ref, dmod_ref, ptot_ref):
        if nm == "b_ada":
            acc = None
            for d in range(N_DEV):
                for s in range(nb):
                    blk = dmod_ref[d * dm + 9 * s:d * dm + 9 * s + 9, :]
                    acc = blk if acc is None else acc + blk
            return jnp.concatenate([acc[k:k + 1, :] for k in range(9)], axis=1)
        if nm == "pool_grp":
            return ptot_ref[...]
        row, lo, n = {"norm_ffn1": (0, 0, D), "norm_mix": (1, 0, D), "norm_ffn2": (2, 0, D), "pool_scale": (3, 0, POOL_W),
                      "q_a_norm": (4, 0, QL), "kv_a_norm": (5, 0, KVL), "q_norm_nope": (6, 0, NOPE),
                      "q_norm_rope": (6, NOPE, ROPE), "k_norm_nope": (7, 0, NOPE), "k_norm_rope": (7, KR_LANE, ROPE)}[nm]
        return red_ref[row:row + 1, lo:lo + n]

    def body(*refs):
        red_ref, dmod_ref, ptot_ref = refs[:3]
        ins = refs[3:3 + 3 * len(names)]
        outs = refs[3 + 3 * len(names):]
        outs[4 * len(names)][...] = red_ref[RED_ROWS - 1:RED_ROWS, LOSS_LANE:]
        for i, nm in enumerate(names):
            g = grad_of(nm, red_ref, dmod_ref, ptot_ref)
            d, mn, vn = _adamw_math(ins[3 * i][...], g, ins[3 * i + 1][...], ins[3 * i + 2][...])
            outs[4 * i][...] = g
            outs[4 * i + 1][...] = d
            outs[4 * i + 2][...] = mn
            outs[4 * i + 3][...] = vn

    flat_in = [a for nm in names for a in params[nm]]
    out_shape = [_sds(params[nm][0].shape, F32) for nm in names for _ in range(4)] + [_sds((1, LANE), F32)]
    res = _pcall(body, name=name, in_specs=[VMEM_SPEC] * (3 + len(flat_in)), out_specs=[VMEM_SPEC] * len(out_shape),
                 out_shape=out_shape)(red, dmod_all, pool_total, *flat_in)
    return {nm: tuple(res[4 * i:4 * i + 4]) for i, nm in enumerate(names)}, res[-1]


def _adamw(w, g, m, v, *, name, comm=None):
    r, c = w.shape
    tr = _pick_rows(r, 256)
    tc = c if tr < r else _pick(c, 256)
    spec = pl.BlockSpec((tr, tc), lambda i, j: (i, j))
    bc1 = 1.0 / (1.0 - ADAM_B1 ** ADAM_STEP)
    bc2 = 1.0 / (1.0 - ADAM_B2 ** ADAM_STEP)

    def body(w_ref, g_ref, m_ref, v_ref, d_ref, mo_ref, vo_ref):
        gv = g_ref[...]
        mn = ADAM_B1 * m_ref[...] + (1.0 - ADAM_B1) * gv
        vn = ADAM_B2 * v_ref[...] + (1.0 - ADAM_B2) * (gv * gv)
        mo_ref[...] = mn
        vo_ref[...] = vn
        d_ref[...] = -ADAM_LR * ((mn * bc1) / (jnp.sqrt(vn * bc2) + ADAM_EPS) + ADAM_WD * w_ref[...])

    out = _sds((r, c), F32)
    return _pcall(body, name=name, out_shape=[out, out, out], grid=(r // tr, c // tc), in_specs=[spec] * 4, out_specs=[spec] * 3,
                  comm=comm)(w, g, m, v)


BIG = ("w_ffn1_in", "w_ffn1_out", "w_in", "w_pool_proj", "w_q_up", "w_kv_up", "w_mla_proj", "w_out", "w_ffn2_in", "w_ffn2_out")
ROW_SHARDED = ("w_ffn1_out", "w_out", "w_ffn2_out")
KERNEL_NAME = {"w_ffn1_in": "ffn1_in", "w_ffn1_out": "ffn1_out", "w_in": "w_in", "w_pool_proj": "pool_proj", "w_q_up": "q_up",
               "w_kv_up": "kv_up", "w_mla_proj": "mla_proj", "w_out": "w_out", "w_ffn2_in": "ffn2_in", "w_ffn2_out": "ffn2_out"}
WEIGHTS = ("w_ada", "b_ada", "norm_ffn1", "w_ffn1_in", "w_ffn1_out", "norm_mix", "w_in", "pool_grp", "pool_scale", "w_pool_proj",
           "q_a_norm", "w_q_up", "kv_a_norm", "w_kv_up", "q_norm_nope", "q_norm_rope", "k_norm_nope", "k_norm_rope", "w_mla_proj",
           "w_out", "norm_ffn2", "w_ffn2_in", "w_ffn2_out")
SMALL = ("b_ada", "norm_ffn1", "norm_mix", "pool_grp", "pool_scale", "q_a_norm", "kv_a_norm", "q_norm_nope", "q_norm_rope",
         "k_norm_nope", "k_norm_rope", "norm_ffn2")


def _assemble(name, stacked):
    if name in ROW_SHARDED:
        return stacked.reshape(stacked.shape[0] * stacked.shape[1], stacked.shape[2])
    return jnp.transpose(stacked, (1, 0, 2)).reshape(stacked.shape[1], stacked.shape[0] * stacked.shape[2])


def _split(name, full):
    if name in ROW_SHARDED:
        return full.reshape(N_CHIP, full.shape[0] // N_CHIP, full.shape[1])
    return jnp.transpose(full.reshape(full.shape[0], N_CHIP, full.shape[1] // N_CHIP), (1, 0, 2))


GU = ("w_ffn1_in", "w_ffn2_in")
NARROW = ("w_in", "w_q_up")
TRANSPOSED = ("w_in",)
W_IN_SHARD = (Z_W - LANE + ROPE) // N_CHIP


MIX_SMALL = ("w_out", "w_pool_proj", "w_mla_proj", "w_q_up", "w_kv_up")
RIDES = {
    "ada_prologue": (("gather", ("w_ffn1_in",)),),
    "ffn1_up_a": (("gather", ("w_ffn1_out",)),),
    "ffn1_up_b": (("gather", ("w_in",)),),
    "mix_in": (("gather", MIX_SMALL),),
    "attn_fwd": (("gather", ("w_ffn2_in", "w_ffn2_out")),),
    "d_ffn2_h": (("swap", ("w_ffn2_out", "w_ffn2_in")),),
    "attn_bwd": (("exchange", ("w_ffn2_out", "w_ffn2_in")),),
    "d_mix_in": (("swap", MIX_SMALL),),
    "d_mix_h": (("exchange", MIX_SMALL), ("swap", ("w_in",))),
    "ffn1_dact": (("exchange", ("w_in",)), ("swap", ("w_ffn1_out",))),
    "d_ffn1_in_a": (("exchange", ("w_ffn1_out",)),),
    "d_ffn1_in_b": (("swap", ("w_ffn1_in@0",)),),
    "d_ffn1_h": (("exchange", ("w_ffn1_in@0",)), ("swap", ("w_ffn1_in@1",))),
    "bwd_norm1": (("exchange", ("w_ffn1_in@1",)),),
    "small_allreduce": (("join", tuple(n for n in BIG if n != "w_ffn1_in") + ("w_ffn1_in@0", "w_ffn1_in@1")),),
}


def _kname(n):
    base, _, part = n.partition("@")
    return KERNEL_NAME[base] + ("@" + part if part else "")


def _both(comms):
    if len(comms) == 1:
        return comms[0]
    ins, outs, sems, aliases, spans = [], [], [], {}, []
    for c in comms:
        spans.append((len(ins), len(c.ins), len(outs), len(c.out_shapes), len(sems), len(c.sems)))
        aliases.update({len(ins) + i: len(outs) + o for i, o in c.aliases.items()})
        ins, outs, sems = ins + c.ins, outs + c.out_shapes, sems + c.sems

    def each(which):
        def run(i_, o_, s_):
            for c, (ia, ni, oa, no, sa, ns) in zip(comms, spans):
                getattr(c, which)(i_[ia:ia + ni], o_[oa:oa + no], s_[sa:sa + ns])
        return run

    return _Comm(ins, outs, sems, each("start"), each("finish"), aliases)


class _ExchangePlan:
    def __init__(self, shards, cidx):
        self.shards, self.cidx = shards, cidx
        self.W, self.G, self.parts, self.pre, self.reduced, self.joined = {}, {}, {}, {}, {}, {}

    def grad(self, key, g):
        self.G[key] = g

    def rider(self, name):
        comms = []
        for kind, names in RIDES.get(name, ()):
            if kind == "gather":
                comms.append(_gather_comm([self.shards[n] for n in names]))
            elif kind == "swap":
                for n in names:
                    self.parts[n] = self._stacked(n)
                comms.append(_swap_comm([self.parts[n] for n in names]))
            elif kind == "exchange":
                comms.append(_exchange_comm([self.pre[n] for n in names]))
            else:
                comms.append(_join_comm([self.reduced[n] for n in names]))
        return _both(comms) if comms else None

    def landed(self, name, outs):
        at = 0
        for kind, names in RIDES[name]:
            for n, o in zip(names, outs[at:at + len(names)]):
                if kind == "gather":
                    self.W[KERNEL_NAME[n]] = self._to_kernel(n, o)
                elif kind == "swap":
                    self.pre[n] = _add_half(self.parts[n], o, self.cidx, name="rs_add_" + _kname(n))
                elif kind == "exchange":
                    self.reduced[n] = _sum_chips(o, self.cidx, name="rs_sum_" + _kname(n))
                else:
                    self.joined[n] = o
            at += len(names)

    @staticmethod
    def _to_kernel(n, stacked):
        if n in GU:
            return stacked
        if n in TRANSPOSED:
            return _w_in_stacked_to_kernel(stacked, W_IN_SHARD)
        full = _assemble(n, stacked)
        return {"w_q_up": _q_up_to_kernel, "w_kv_up": _kv_up_to_kernel}.get(n, lambda w: w)(full)

    def _stacked(self, n):
        g = self.G[_kname(n)]
        if n.partition("@")[0] in GU:
            return g
        if n in TRANSPOSED:
            return _w_in_kernel_to_stacked(g, W_IN_SHARD)
        full = {"w_q_up": _q_up_from_kernel, "w_kv_up": _kv_up_from_kernel}.get(n, lambda w: w)(g)
        return _split(n, full)


def kernel(x, c, positions, w_ada, b_ada, norm_ffn1, w_ffn1_in, w_ffn1_out, norm_mix, w_in, pool_grp, pool_scale, w_pool_proj, q_a_norm, w_q_up, kv_a_norm, w_kv_up, q_norm_nope, q_norm_rope, k_norm_nope, k_norm_rope, w_mla_proj, w_out, norm_ffn2, w_ffn2_in, w_ffn2_out, loss_target, m_w_ada, m_b_ada, m_norm_ffn1, m_w_ffn1_in, m_w_ffn1_out, m_norm_mix, m_w_in, m_pool_grp, m_pool_scale, m_w_pool_proj, m_q_a_norm, m_w_q_up, m_kv_a_norm, m_w_kv_up, m_q_norm_nope, m_q_norm_rope, m_k_norm_nope, m_k_norm_rope, m_w_mla_proj, m_w_out, m_norm_ffn2, m_w_ffn2_in, m_w_ffn2_out, v_w_ada, v_b_ada, v_norm_ffn1, v_w_ffn1_in, v_w_ffn1_out, v_norm_mix, v_w_in, v_pool_grp, v_pool_scale, v_w_pool_proj, v_q_a_norm, v_w_q_up, v_kv_a_norm, v_w_kv_up, v_q_norm_nope, v_q_norm_rope, v_k_norm_nope, v_k_norm_rope, v_w_mla_proj, v_w_out, v_norm_ffn2, v_w_ffn2_in, v_w_ffn2_out):
    args = dict(locals())
    wts = {n: args[n][0] for n in WEIGHTS}
    mom = {n: args["m_" + n][0] for n in WEIGHTS}
    var = {n: args["v_" + n][0] for n in WEIGHTS}
    nb, seq, dm = x.shape
    tokens = nb * seq
    xi, yi, ci = _coords()
    chip = 2 * xi + yi

    cidx = ci.astype(jnp.int32).reshape(1)
    late = tuple(n for n in BIG if n != "w_ffn1_in" and n not in TRANSPOSED)
    plan = _ExchangePlan({n: _pad_rows(wts[n].T.astype(MX)) if n in TRANSPOSED else wts[n].astype(MX)
                          for n in BIG if n not in late}, cidx)
    plan.W["pool_grp"] = wts["pool_grp"].astype(MX)

    ncol = w_ada.shape[2]
    b_cols = lax.dynamic_slice_in_dim(wts["b_ada"].reshape(1, -1), chip * ncol, ncol, axis=1)
    c_slots, mod_slots, *rounded = _run(plan, _ada_prologue, c, wts["w_ada"], b_cols, [wts[n] for n in late], name="ada_prologue")
    plan.shards.update(zip(late, rounded))
    c_all = c_slots[:, :nb].reshape(N_DEV * nb, dm)
    mod3 = jnp.transpose(mod_slots[:, :nb], (1, 0, 2)).reshape(nb, 9, dm)
    P = {"norm_ffn1": wts["norm_ffn1"].reshape(1, dm), "norm_mix": wts["norm_mix"].reshape(1, dm),
         "norm_ffn2": wts["norm_ffn2"].reshape(1, dm), "pool_scale": wts["pool_scale"].reshape(1, POOL_W),
         "q_a_norm": wts["q_a_norm"].reshape(1, QL), "kv_a_norm": wts["kv_a_norm"].reshape(1, KVL),
         "q_gain": _gain_slab(wts["q_norm_nope"].reshape(1, NOPE), wts["q_norm_rope"].reshape(1, ROPE)),
         "k_gain": _gain_slab(wts["k_norm_nope"].reshape(1, NOPE), wts["k_norm_rope"].reshape(1, ROPE))}
    cos, sin = _rope_tables(positions.reshape(tokens))

    loss8, grad_x, stats, d_pool_grp = _layer_fwd_bwd(x.reshape(tokens, dm), loss_target.reshape(tokens, dm), mod3, cos, sin, plan, P)

    slab = _pack_stats(*stats, loss8, name="pack_stats")
    red, dmod_rows, pool_total = _run(plan, _small_allreduce, slab, d_pool_grp.reshape(POOL_G * LANE, LANE), name="small_allreduce")
    grads_t = {n: plan.joined[n][:W_IN_SHARD] for n in TRANSPOSED}
    grads = {n: grads_t[n].T if n in TRANSPOSED else plan.joined[n] for n in BIG if n != "w_ffn1_in"}
    grads["w_ffn1_in"] = jnp.concatenate([plan.joined["w_ffn1_in@0"], plan.joined["w_ffn1_in@1"]], axis=0)
    dm_rows = dmod_rows.shape[0] // N_DEV
    dmod_all = dmod_rows.reshape(N_DEV, dm_rows, SLAB_W)[:, :9 * nb].reshape(N_DEV * nb, 9 * dm)
    dmod_cols = lax.dynamic_slice_in_dim(dmod_all, chip * ncol, ncol, axis=1)
    grads["w_ada"] = _ada_bwd(c_all, dmod_cols, name="ada_bwd")

    delta, new_m, new_v = {}, {}, {}
    as2d = lambda a: a.reshape(POOL_G * LANE, LANE) if a.ndim == 4 else a.reshape(1, -1)
    upd, loss_row = _small_update(red, dmod_rows, pool_total, nb,
                                  {n: tuple(as2d(args[p + n]) for p in ("", "m_", "v_")) for n in SMALL}, name="small_update")
    loss = loss_row[0, 0]
    for n in SMALL:
        grads[n], delta[n], new_m[n], new_v[n] = upd[n]
    for n in ("w_ada",) + BIG:
        if n in NARROW:
            res = _adamw(wts[n].T, grads_t[n] if n in TRANSPOSED else grads[n].T, mom[n].T, var[n].T, name="adamw_" + n)
            delta[n], new_m[n], new_v[n] = (r.T for r in res)
        else:
            delta[n], new_m[n], new_v[n] = _adamw(wts[n], grads[n], mom[n], var[n], name="adamw_" + n)

    def lead(a, n):
        return a.reshape((1,) + wts[n].shape)

    return (loss, grad_x.reshape(nb, seq, dm), *[lead(grads[n], n) for n in WEIGHTS], *[lead(delta[n], n) for n in WEIGHTS],
            *[lead(new_m[n], n) for n in WEIGHTS], *[lead(new_v[n], n) for n in WEIGHTS])
```

```python
import functools
import math

import jax
import jax.numpy as jnp
from jax import lax
from jax.experimental import pallas as pl
from jax.experimental.pallas import tpu as pltpu

F32 = jnp.float32
MX = jnp.bfloat16

D = 1024
DFF = 2816
NH = 8
POOL_W = 512
POOL_G = 4
QL = 384
KVL = 256
ROPE = 32
NOPE = 64
LANE = 128
SUBLANE = 8
EPS = 1e-6
ATTN_SCALE = 1.0 / math.sqrt(96.0)
NEG = -1e30

Z_UP, Z_QL, Z_KV, Z_KR, Z_GP, Z_GM, Z_W = 0, 512, 896, 1152, 1280, 2304, 3328
KR_LANE = 64
LAT_W = 1280

ADAM_LR, ADAM_B1, ADAM_B2, ADAM_EPS, ADAM_WD, ADAM_STEP = 0.001, 0.9, 0.999, 1e-08, 0.01, 10

VMEM_LIMIT = 48 * 1024 * 1024
MESH = pl.DeviceIdType.MESH
N_DEV = 8
N_CHIP = 4


class _Comm:
    def __init__(self, ins, out_shapes, sems, start, finish, aliases=None):
        self.ins, self.out_shapes, self.sems = list(ins), list(out_shapes), list(sems)
        self.start, self.finish = start, finish
        self.aliases = aliases or {}


def _pcall(body, *, name, out_shape, grid=(), in_specs=None, out_specs=None, scratch=(), grid_spec=None, aliases=None,
           comm=None):
    params = pltpu.CompilerParams(vmem_limit_bytes=VMEM_LIMIT)
    kw = dict(name=name, compiler_params=params)
    if comm is None:
        if aliases:
            kw["input_output_aliases"] = aliases
        if grid_spec is not None:
            return pl.pallas_call(body, grid_spec=grid_spec, out_shape=out_shape, **kw)
        return pl.pallas_call(body, grid=grid, in_specs=in_specs, out_specs=out_specs, scratch_shapes=scratch,
                              out_shape=out_shape, **kw)
    single = not isinstance(out_shape, (list, tuple))
    outs = [out_shape] if single else list(out_shape)
    ospecs = [out_specs] if single else list(out_specs)
    n_in, n_out, n_ci, n_co, n_scr = len(in_specs), len(outs), len(comm.ins), len(comm.out_shapes), len(scratch)
    io = dict(aliases or {})
    io.update({n_in + i: n_out + o for i, o in comm.aliases.items()})

    def riding(*refs):
        ins, cins = refs[:n_in], refs[n_in:n_in + n_ci]
        at = n_in + n_ci
        os_, couts = refs[at:at + n_out], refs[at + n_out:at + n_out + n_co]
        at += n_out + n_co
        scr, csems = refs[at:at + n_scr], refs[at + n_scr:]
        if grid:
            first = functools.reduce(jnp.logical_and, [pl.program_id(d) == 0 for d in range(len(grid))])
            last = functools.reduce(jnp.logical_and, [pl.program_id(d) == grid[d] - 1 for d in range(len(grid))])
            pl.when(first)(lambda: comm.start(cins, couts, csems))
            body(*ins, *os_, *scr)
            pl.when(last)(lambda: comm.finish(cins, couts, csems))
        else:
            comm.start(cins, couts, csems)
            body(*ins, *os_, *scr)
            comm.finish(cins, couts, csems)

    call = pl.pallas_call(riding, grid=grid, in_specs=list(in_specs) + [HBM_SPEC] * n_ci, out_specs=ospecs + [HBM_SPEC] * n_co,
                          out_shape=outs + comm.out_shapes, scratch_shapes=list(scratch) + comm.sems,
                          input_output_aliases=io, **kw)

    def run(*args):
        res = call(*args, *comm.ins)
        main = list(res[:n_out])
        return (main[0] if single else main), list(res[n_out:])

    return run


def _pick(dim, target):
    best = None
    for t in range(LANE, min(dim, target) + 1, LANE):
        if dim % t == 0:
            best = t
    return dim if best is None else best


def _sds(shape, dtype):
    return jax.ShapeDtypeStruct(shape, dtype)


def _dw(a, g, *, name, tm=512, tn=1024, out_t=False, comm=None):
    return _mm(a, g, mode="tn", out_dtype=F32, name=name, tm=tm, tn=tn, tk=a.shape[0], n_outer=True, out_t=out_t, comm=comm)


def _mm(a, b, *, mode, out_dtype, name, tm=1024, tn=1024, tk=4096, n_outer=False, out_t=False, comm=None):
    if mode == "nn":
        (M, K), (K2, N) = a.shape, b.shape
    elif mode == "nt":
        (M, K), (N, K2) = a.shape, b.shape
    else:
        (K, M), (K2, N) = a.shape, b.shape
    assert K == K2, (name, a.shape, b.shape)
    tm, tn, tk = _pick(M, tm), _pick(N, tn), _pick(K, tk)
    nk = K // tk
    if n_outer:
        ij = lambda g0, g1: (g1, g0)
        grid = (N // tn, M // tm, nk)
    else:
        ij = lambda g0, g1: (g0, g1)
        grid = (M // tm, N // tn, nk)
    if mode == "tn":
        a_spec = pl.BlockSpec((tk, tm), lambda g0, g1, k: (k, ij(g0, g1)[0]))
    else:
        a_spec = pl.BlockSpec((tm, tk), lambda g0, g1, k: (ij(g0, g1)[0], k))
    if mode == "nt":
        b_spec = pl.BlockSpec((tn, tk), lambda g0, g1, k: (ij(g0, g1)[1], k))
    else:
        b_spec = pl.BlockSpec((tk, tn), lambda g0, g1, k: (k, ij(g0, g1)[1]))
    if out_t:
        assert nk == 1, name
        o_spec = pl.BlockSpec((tn, tm), lambda g0, g1, k: ij(g0, g1)[::-1])
    else:
        o_spec = pl.BlockSpec((tm, tn), lambda g0, g1, k: ij(g0, g1))
    dn = {"nn": (((1,), (0,)), ((), ())), "nt": (((1,), (1,)), ((), ())), "tn": (((0,), (0,)), ((), ()))}[mode]

    def dot(a_ref, b_ref):
        return lax.dot_general(a_ref[...].astype(MX), b_ref[...].astype(MX), dn, preferred_element_type=F32)

    def body_one(a_ref, b_ref, o_ref):
        prod = dot(a_ref, b_ref)
        o_ref[...] = (prod.T if out_t else prod).astype(o_ref.dtype)

    def body_acc(a_ref, b_ref, o_ref, acc_ref):
        k = pl.program_id(2)
        part = dot(a_ref, b_ref)

        @pl.when(k == 0)
        def _():
            acc_ref[...] = part

        @pl.when(k > 0)
        def _():
            acc_ref[...] += part

        @pl.when(k == nk - 1)
        def _():
            o_ref[...] = acc_ref[...].astype(o_ref.dtype)

    return _pcall(body_one if nk == 1 else body_acc, name=name, out_shape=_sds((N, M) if out_t else (M, N), out_dtype), grid=grid,
                  in_specs=[a_spec, b_spec], out_specs=o_spec, scratch=[] if nk == 1 else [pltpu.VMEM((tm, tn), F32)],
                  comm=comm)(a, b)


def _gu_shard(q):
    return (q % 2) * 2 + q // 2


def _ffn_up(h, w_st, *, name, tm=512, comm=None):
    T, dm = h.shape
    hw = w_st.shape[2]
    tm = _pick(T, tm)

    def body(h_ref, wg_ref, wu_ref, gu_ref, a_ref):
        hv = h_ref[...]
        g = jnp.dot(hv, wg_ref[0], preferred_element_type=F32)
        u = jnp.dot(hv, wu_ref[0], preferred_element_type=F32)
        gu_ref[:, :hw] = g.astype(gu_ref.dtype)
        gu_ref[:, hw:] = u.astype(gu_ref.dtype)
        a_ref[...] = (g * _sigmoid(g) * u).astype(a_ref.dtype)

    return _pcall(body, name=name, grid=(T // tm, 2),
                  in_specs=[pl.BlockSpec((tm, dm), lambda i, j: (i, 0)), pl.BlockSpec((1, dm, hw), lambda i, j: (j, 0, 0)),
                            pl.BlockSpec((1, dm, hw), lambda i, j: (2 + j, 0, 0))],
                  out_specs=[pl.BlockSpec((tm, 2 * hw), lambda i, j: (i, j)), pl.BlockSpec((tm, hw), lambda i, j: (i, j))],
                  out_shape=[_sds((T, 4 * hw), MX), _sds((T, 2 * hw), MX)], comm=comm)(h, w_st, w_st)


def _ffn_up_first(x, mod3, gain, w_st, *, sub, name, tm=512, comm=None):
    T, dm = x.shape
    hw = w_st.shape[2]
    tm = _pick(T, tm)
    tps = (T // mod3.shape[0]) // tm

    def body(x_ref, mod_ref, n_ref, wg_ref, wu_ref, h_ref, gu_ref, a_ref):
        xv = x_ref[...]
        xn = xv * _rsq(xv) * n_ref[...]
        hv = (xn * (1.0 + mod_ref[0, 3 * sub + 1:3 * sub + 2, :]) + mod_ref[0, 3 * sub:3 * sub + 1, :]).astype(h_ref.dtype)
        h_ref[...] = hv
        g = jnp.dot(hv, wg_ref[0], preferred_element_type=F32)
        u = jnp.dot(hv, wu_ref[0], preferred_element_type=F32)
        gu_ref[:, :hw] = g.astype(gu_ref.dtype)
        gu_ref[:, hw:] = u.astype(gu_ref.dtype)
        a_ref[...] = (g * _sigmoid(g) * u).astype(a_ref.dtype)

    return _pcall(body, name=name, grid=(T // tm,),
                  in_specs=[_row_spec(tm, dm), pl.BlockSpec((1, 9, dm), lambda i: (i // tps, 0, 0)),
                            pl.BlockSpec((1, dm), lambda i: (0, 0)), pl.BlockSpec((1, dm, hw), lambda i: (0, 0, 0)),
                            pl.BlockSpec((1, dm, hw), lambda i: (2, 0, 0))],
                  out_specs=[_row_spec(tm, dm), pl.BlockSpec((tm, 2 * hw), lambda i: (i, 0)), pl.BlockSpec((tm, hw), lambda i: (i, 0))],
                  out_shape=[_sds((T, dm), MX), _sds((T, 4 * hw), MX), _sds((T, 2 * hw), MX)],
                  comm=comm)(x, mod3, gain, w_st, w_st)


def _ffn_up_second(h, w_st, gu, a, *, name, tm=512, comm=None):
    T, dm = h.shape
    hw = w_st.shape[2]
    tm = _pick(T, tm)

    def body(h_ref, wg_ref, wu_ref, gu_in, a_in, gu_ref, a_ref):
        hv = h_ref[...]
        g = jnp.dot(hv, wg_ref[0], preferred_element_type=F32)
        u = jnp.dot(hv, wu_ref[0], preferred_element_type=F32)
        gu_ref[:, :hw] = g.astype(gu_ref.dtype)
        gu_ref[:, hw:] = u.astype(gu_ref.dtype)
        a_ref[...] = (g * _sigmoid(g) * u).astype(a_ref.dtype)

    return _pcall(body, name=name, grid=(T // tm,),
                  in_specs=[_row_spec(tm, dm), pl.BlockSpec((1, dm, hw), lambda i: (1, 0, 0)),
                            pl.BlockSpec((1, dm, hw), lambda i: (3, 0, 0)), HBM_SPEC, HBM_SPEC],
                  out_specs=[pl.BlockSpec((tm, 2 * hw), lambda i: (i, 1)), pl.BlockSpec((tm, hw), lambda i: (i, 1))],
                  out_shape=[_sds(gu.shape, gu.dtype), _sds(a.shape, a.dtype)], aliases={3: 0, 4: 1},
                  comm=comm)(h, w_st, w_st, gu, a)


def _ffn_dact(df, gu, w_out, *, name, tm=512, comm=None):
    T, dm = df.shape
    hw = gu.shape[1] // 4
    tm = _pick(T, tm)

    def body(df_ref, gu_ref, wo_ref, dgu_ref):
        da = lax.dot_general(df_ref[...], wo_ref[...], (((1,), (1,)), ((), ())), preferred_element_type=F32)
        g = gu_ref[:, :hw].astype(F32)
        u = gu_ref[:, hw:].astype(F32)
        s = _sigmoid(g)
        dgu_ref[:, :hw] = (da * u * (s * (1.0 + g * (1.0 - s)))).astype(dgu_ref.dtype)
        dgu_ref[:, hw:] = (da * (g * s)).astype(dgu_ref.dtype)

    return _pcall(body, name=name, grid=(T // tm, 2),
                  in_specs=[pl.BlockSpec((tm, dm), lambda i, j: (i, 0)), pl.BlockSpec((tm, 2 * hw), lambda i, j: (i, j)),
                            pl.BlockSpec((hw, dm), lambda i, j: (j, 0))],
                  out_specs=pl.BlockSpec((tm, 2 * hw), lambda i, j: (i, j)), out_shape=_sds(gu.shape, MX),
                  comm=comm)(df, gu, w_out)


def _ffn_dh(dgu, w_st, *, name, tm=1024, comm=None):
    T = dgu.shape[0]
    _, dm, hw = w_st.shape
    tm = _pick(T, tm)

    def body(d_ref, w_ref, o_ref, acc_ref):
        q = pl.program_id(1)
        part = lax.dot_general(d_ref[...], w_ref[0], (((1,), (1,)), ((), ())), preferred_element_type=F32)

        @pl.when(q == 0)
        def _():
            acc_ref[...] = part

        @pl.when(jnp.logical_and(q > 0, q < 3))
        def _():
            acc_ref[...] += part

        @pl.when(q == 3)
        def _():
            o_ref[...] = acc_ref[...] + part

    return _pcall(body, name=name, grid=(T // tm, 4),
                  in_specs=[pl.BlockSpec((tm, hw), lambda i, q: (i, q)), pl.BlockSpec((1, dm, hw), lambda i, q: (_gu_shard(q), 0, 0))],
                  out_specs=pl.BlockSpec((tm, dm), lambda i, q: (i, 0)), out_shape=_sds((T, dm), F32),
                  scratch=[pltpu.VMEM((tm, dm), F32)], comm=comm)(dgu, w_st)


def _ffn_dw_in(h, dgu, *, name, rows=None, tm=512, comm=None):
    T, dm = h.shape
    hw = dgu.shape[1] // 4
    first, count = rows if rows is not None else (0, dm)
    tm = _pick(count, tm)
    skip = first // tm

    def body(h_ref, d_ref, o_ref):
        o_ref[0] = lax.dot_general(h_ref[...], d_ref[...], (((0,), (0,)), ((), ())), preferred_element_type=F32)

    return _pcall(body, name=name, grid=(4, count // tm),
                  in_specs=[pl.BlockSpec((T, tm), lambda q, i: (0, skip + i)), pl.BlockSpec((T, hw), lambda q, i: (0, q))],
                  out_specs=pl.BlockSpec((1, tm, hw), lambda q, i: (_gu_shard(q), i, 0)),
                  out_shape=_sds((4, count, hw), F32), comm=comm)(h, dgu)


def _rsq(x):
    return lax.rsqrt(jnp.mean(x * x, axis=-1, keepdims=True) + EPS)


def _sigmoid(x):
    return 1.0 / (1.0 + jnp.exp(-x))


def _row_spec(tm, w):
    return pl.BlockSpec((tm, w), lambda i: (i, 0))


def _mm_resid_norm(a, w, x_prev, mod3, gain, *, sub, coef, name, tm=512, comm=None):
    T, k = a.shape
    dm = w.shape[1]
    tm = _pick(T, tm)
    tps = (T // mod3.shape[0]) // tm

    def body(a_ref, w_ref, x_ref, mod_ref, n_ref, f_ref, xo_ref, h_ref):
        f = jnp.dot(a_ref[...], w_ref[...], preferred_element_type=F32)
        f_ref[...] = f
        x = x_ref[...] + coef * mod_ref[0, 3 * sub - 1:3 * sub, :] * f
        xo_ref[...] = x
        xn = x * _rsq(x) * n_ref[...]
        h_ref[...] = (xn * (1.0 + mod_ref[0, 3 * sub + 1:3 * sub + 2, :]) + mod_ref[0, 3 * sub:3 * sub + 1, :]).astype(h_ref.dtype)

    row = _row_spec(tm, dm)
    return _pcall(body, name=name, grid=(T // tm,),
                  in_specs=[_row_spec(tm, k), pl.BlockSpec((k, dm), lambda i: (0, 0)), row,
                            pl.BlockSpec((1, 9, dm), lambda i: (i // tps, 0, 0)), pl.BlockSpec((1, dm), lambda i: (0, 0))],
                  out_specs=[row, row, row], out_shape=[_sds((T, dm), F32), _sds((T, dm), F32), _sds((T, dm), MX)],
                  comm=comm)(a, w, x_prev, mod3, gain)


def _mm_loss(a, w, x2, tgt, mod3, *, name, tm=512):
    T, k = a.shape
    dm = w.shape[1]
    tm = _pick(T, tm)
    tps = (T // mod3.shape[0]) // tm
    mod_spec = pl.BlockSpec((1, 9, dm), lambda i: (i // tps, 0, 0))
    row = _row_spec(tm, dm)
    stat_spec = pl.BlockSpec((1, SUBLANE, dm), lambda i: (i // tps, 0, 0))
    loss_spec = pl.BlockSpec((SUBLANE, LANE), lambda i: (0, 0))

    def body(a_ref, w_ref, x_ref, t_ref, mod_ref, dy_ref, df_ref, st_ref, loss_ref):
        i = pl.program_id(0)
        g = mod_ref[0, 8:9, :]
        f = jnp.dot(a_ref[...], w_ref[...], preferred_element_type=F32)
        err = x_ref[...] + 0.5 * g * f - t_ref[...]
        dy = err * (1.0 / dm)
        dy_ref[...] = dy
        df_ref[...] = (0.5 * g * dy).astype(df_ref.dtype)
        dgate = jnp.sum(0.5 * dy * f, axis=0, keepdims=True)
        part = 0.5 * jnp.sum(jnp.sum(err * err, axis=0, keepdims=True), axis=1, keepdims=True) * (1.0 / dm)

        @pl.when(i % tps == 0)
        def _():
            st_ref[...] = jnp.zeros_like(st_ref)

        @pl.when(i == 0)
        def _():
            loss_ref[...] = jnp.zeros_like(loss_ref)

        st_ref[0, 0:1, :] += dgate
        loss_ref[...] += jnp.broadcast_to(part, loss_ref.shape)

    return _pcall(body, name=name, grid=(T // tm,),
                  in_specs=[_row_spec(tm, k), pl.BlockSpec((k, dm), lambda i: (0, 0)), row, row, mod_spec],
                  out_specs=[row, row, stat_spec, loss_spec],
                  out_shape=[_sds((T, dm), F32), _sds((T, dm), MX), _sds((mod3.shape[0], SUBLANE, dm), F32),
                             _sds((SUBLANE, LANE), F32)])(a, w, x2, tgt, mod3)


def _norm_bwd_tail(dhv, x_ref, dxi_ref, f_ref, mod_ref, n_ref, dx_ref, df_ref, st_ref, *, first, sub, coef):
    x = x_ref[...]
    r = _rsq(x)
    xhat = x * r
    n = n_ref[...]
    d_shift = jnp.sum(dhv, axis=0, keepdims=True)
    d_scale = jnp.sum(dhv * (xhat * n), axis=0, keepdims=True)
    dxn = dhv * (1.0 + mod_ref[0, 3 * sub + 1:3 * sub + 2, :])
    d_gain = jnp.sum(dxn * xhat, axis=0, keepdims=True)
    dxhat = dxn * n
    dx = dxi_ref[...] + r * (dxhat - xhat * jnp.mean(dxhat * xhat, axis=-1, keepdims=True))
    dx_ref[...] = dx

    @pl.when(first)
    def _():
        st_ref[...] = jnp.zeros_like(st_ref)

    st_ref[0, 0:1, :] += d_shift
    st_ref[0, 1:2, :] += d_scale
    st_ref[0, 2:3, :] += d_gain
    if f_ref is not None:
        st_ref[0, 3:4, :] += jnp.sum(coef * dx * f_ref[...], axis=0, keepdims=True)
        df_ref[...] = (coef * mod_ref[0, 3 * sub - 1:3 * sub, :] * dx).astype(df_ref.dtype)


def _ffn_dh_norm(dgu, w_st, x_cur, dx_in, f_prev, mod3, gain, *, sub, coef, name, tm=512, comm=None):
    T = dgu.shape[0]
    _, dm, hw = w_st.shape
    tm = _pick(T, tm)
    nb = mod3.shape[0]
    tps = (T // nb) // tm

    def body(d_ref, w_ref, x_ref, dxi_ref, f_ref, mod_ref, n_ref, dx_ref, df_ref, st_ref, acc_ref):
        i = pl.program_id(0)
        q = pl.program_id(1)
        part = lax.dot_general(d_ref[...], w_ref[0], (((1,), (1,)), ((), ())), preferred_element_type=F32)

        @pl.when(q == 0)
        def _():
            acc_ref[...] = part

        @pl.when(q > 0)
        def _():
            acc_ref[...] += part

        @pl.when(q == 3)
        def _():
            _norm_bwd_tail(acc_ref[...], x_ref, dxi_ref, f_ref, mod_ref, n_ref, dx_ref, df_ref, st_ref,
                           first=i % tps == 0, sub=sub, coef=coef)

    row = pl.BlockSpec((tm, dm), lambda i, q: (i, 0))
    return _pcall(body, name=name, grid=(T // tm, 4),
                  in_specs=[pl.BlockSpec((tm, hw), lambda i, q: (i, q)), pl.BlockSpec((1, dm, hw), lambda i, q: (_gu_shard(q), 0, 0)),
                            row, row, row, pl.BlockSpec((1, 9, dm), lambda i, q: (i // tps, 0, 0)),
                            pl.BlockSpec((1, dm), lambda i, q: (0, 0))],
                  out_specs=[row, row, pl.BlockSpec((1, SUBLANE, dm), lambda i, q: (i // tps, 0, 0))],
                  out_shape=[_sds((T, dm), F32), _sds((T, dm), MX), _sds((nb, SUBLANE, dm), F32)],
                  scratch=[pltpu.VMEM((tm, dm), F32)], comm=comm)(dgu, w_st, x_cur, dx_in, f_prev, mod3, gain)


def _mm_norm_bwd(a, b, x_cur, dx_in, f_prev, mod3, gain, *, sub, coef, name, tm=512, comm=None):
    T, k = a.shape
    dm = b.shape[1]
    tm = _pick(T, tm)
    nb = mod3.shape[0]
    tps = (T // nb) // tm

    def body(a_ref, b_ref, x_ref, dxi_ref, f_ref, mod_ref, n_ref, dx_ref, df_ref, st_ref):
        dh = jnp.dot(a_ref[...], b_ref[...], preferred_element_type=F32)
        _norm_bwd_tail(dh, x_ref, dxi_ref, f_ref, mod_ref, n_ref, dx_ref, df_ref, st_ref,
                       first=pl.program_id(0) % tps == 0, sub=sub, coef=coef)

    row = _row_spec(tm, dm)
    return _pcall(body, name=name, grid=(T // tm,),
                  in_specs=[_row_spec(tm, k), pl.BlockSpec((k, dm), lambda i: (0, 0)), row, row, row,
                            pl.BlockSpec((1, 9, dm), lambda i: (i // tps, 0, 0)), pl.BlockSpec((1, dm), lambda i: (0, 0))],
                  out_specs=[row, row, pl.BlockSpec((1, SUBLANE, dm), lambda i: (i // tps, 0, 0))],
                  out_shape=[_sds((T, dm), F32), _sds((T, dm), MX), _sds((nb, SUBLANE, dm), F32)],
                  comm=comm)(a, b, x_cur, dx_in, f_prev, mod3, gain)


def _bwd_block(x_cur, dh, dx_in, f_prev, mod3, gain, *, sub, coef, name, tm=256, comm=None):
    T, dm = x_cur.shape
    nb = mod3.shape[0]
    tps = (T // nb) // tm
    has_f = f_prev is not None
    mod_spec = pl.BlockSpec((1, 9, dm), lambda i: (i // tps, 0, 0))
    vec_spec = pl.BlockSpec((1, dm), lambda i: (0, 0))
    stat_spec = pl.BlockSpec((1, SUBLANE, dm), lambda i: (i // tps, 0, 0))
    row = _row_spec(tm, dm)

    def body(*refs):
        if has_f:
            x_ref, dh_ref, dxi_ref, f_ref, mod_ref, n_ref, dx_ref, df_ref, st_ref = refs
        else:
            x_ref, dh_ref, dxi_ref, mod_ref, n_ref, dx_ref, st_ref = refs
            f_ref = df_ref = None
        _norm_bwd_tail(dh_ref[...], x_ref, dxi_ref, f_ref, mod_ref, n_ref, dx_ref, df_ref, st_ref,
                       first=pl.program_id(0) % tps == 0, sub=sub, coef=coef)

    st_shape = _sds((nb, SUBLANE, dm), F32)
    if has_f:
        return _pcall(body, name=name, grid=(T // tm,), in_specs=[row, row, row, row, mod_spec, vec_spec],
                      out_specs=[row, row, stat_spec],
                      out_shape=[_sds((T, dm), F32), _sds((T, dm), MX), st_shape], comm=comm)(x_cur, dh, dx_in, f_prev, mod3, gain)
    return _pcall(body, name=name, grid=(T // tm,), in_specs=[row, row, row, mod_spec, vec_spec],
                  out_specs=[row, stat_spec], out_shape=[_sds((T, dm), F32), st_shape], comm=comm)(x_cur, dh, dx_in, mod3, gain)


def _mix_out_fwd(z, br_pool, attn, w_mla, w_out, x_prev, mod3, gain, *, sub, name, tm=512):
    T = z.shape[0]
    dm = w_out.shape[1]
    tm = _pick(T, tm)
    tps = (T // mod3.shape[0]) // tm
    whole = lambda a: pl.BlockSpec(a.shape, lambda i: (0, 0))

    def body(z_ref, bp_ref, at_ref, wm_ref, wo_ref, x_ref, mod_ref, n_ref, bm_ref, mg_ref, mo_ref, xo_ref, h_ref):
        bm = jnp.dot(at_ref[...], wm_ref[...], preferred_element_type=F32).astype(MX)
        bm_ref[...] = bm
        gp = z_ref[:, Z_GP:Z_GP + D].astype(F32)
        gm = z_ref[:, Z_GM:Z_GM + D].astype(F32)
        merged = (_sigmoid(gp) * bp_ref[...] + _sigmoid(gm) * bm).astype(MX)
        mg_ref[...] = merged
        mo = jnp.dot(merged, wo_ref[...], preferred_element_type=F32)
        mo_ref[...] = mo
        x = x_ref[...] + mod_ref[0, 3 * sub - 1:3 * sub, :] * mo
        xo_ref[...] = x
        xn = x * _rsq(x) * n_ref[...]
        h_ref[...] = (xn * (1.0 + mod_ref[0, 3 * sub + 1:3 * sub + 2, :]) + mod_ref[0, 3 * sub:3 * sub + 1, :]).astype(h_ref.dtype)

    row = _row_spec(tm, dm)
    return _pcall(body, name=name, grid=(T // tm,),
                  in_specs=[_row_spec(tm, Z_W), row, _row_spec(tm, attn.shape[1]), whole(w_mla), whole(w_out), row,
                            pl.BlockSpec((1, 9, dm), lambda i: (i // tps, 0, 0)), pl.BlockSpec((1, dm), lambda i: (0, 0))],
                  out_specs=[row, row, row, row, row],
                  out_shape=[_sds((T, dm), MX), _sds((T, dm), MX), _sds((T, dm), F32), _sds((T, dm), F32), _sds((T, dm), MX)],
                  )(z, br_pool, attn, w_mla, w_out, x_prev, mod3, gain)


def _mix_out_bwd(dmo, w_out, z, br_pool, br_mla, w_mla, *, name, tm=512):
    T = z.shape[0]
    tm = _pick(T, tm)
    whole = lambda a: pl.BlockSpec(a.shape, lambda i: (0, 0))
    nt = (((1,), (1,)), ((), ()))

    def body(dmo_ref, wo_ref, z_ref, bp_ref, bm_ref, wm_ref, dbp_ref, dbm_ref, dg_ref, dat_ref):
        dm = lax.dot_general(dmo_ref[...], wo_ref[...], nt, preferred_element_type=F32)
        sp = _sigmoid(z_ref[:, Z_GP:Z_GP + D].astype(F32))
        sm = _sigmoid(z_ref[:, Z_GM:Z_GM + D].astype(F32))
        dbp_ref[...] = (dm * sp).astype(dbp_ref.dtype)
        dbm = (dm * sm).astype(MX)
        dbm_ref[...] = dbm
        dg_ref[:, :D] = (dm * bp_ref[...].astype(F32) * sp * (1.0 - sp)).astype(dg_ref.dtype)
        dg_ref[:, D:] = (dm * bm_ref[...].astype(F32) * sm * (1.0 - sm)).astype(dg_ref.dtype)
        dat_ref[...] = lax.dot_general(dbm, wm_ref[...], nt, preferred_element_type=F32)

    row = _row_spec(tm, D)
    return _pcall(body, name=name, grid=(T // tm,),
                  in_specs=[row, whole(w_out), _row_spec(tm, Z_W), row, row, whole(w_mla)],
                  out_specs=[row, row, _row_spec(tm, 2 * D), _row_spec(tm, w_mla.shape[0])],
                  out_shape=[_sds((T, D), MX), _sds((T, D), MX), _sds((T, 2 * D), MX), _sds((T, w_mla.shape[0]), F32)],
                  )(dmo, w_out, z, br_pool, br_mla, w_mla)


def _shift_down(x, k, row):
    return jnp.where(row >= k, pltpu.roll(x, k, 0), 0.0)


def _shift_up(x, k, row, n):
    return jnp.where(row < n - k, pltpu.roll(x, n - k, 0), 0.0)


def _pool_fwd(z, pool_grp, pool_scale, *, nb, name):
    T = z.shape[0]
    S = T // nb
    blk = pl.BlockSpec((S, LANE), lambda b, g: (b, g))

    def body(u_ref, w_ref, s_ref, pooled_ref, mixed_ref, scaled_ref):
        g = pl.program_id(1)
        u = u_ref[...].astype(F32)
        row = lax.broadcasted_iota(jnp.int32, u.shape, 0)
        s2 = u + _shift_down(u, 1, row)
        s4 = s2 + _shift_down(s2, 2, row)
        s8 = s4 + _shift_down(s4, 4, row)
        s16 = s8 + _shift_down(s8, 8, row)
        win = jnp.where(g == 0, s2, jnp.where(g == 1, s4, jnp.where(g == 2, s8, s16)))
        width = lax.shift_left(jnp.int32(2), g)
        cnt = jnp.minimum(row + 1, width).astype(F32)
        pooled = (win / cnt - u).astype(MX)
        pooled_ref[...] = pooled
        mixed = jnp.dot(pooled, w_ref[0], preferred_element_type=F32)
        mixed_ref[...] = mixed
        scaled_ref[...] = (mixed * s_ref[...]).astype(scaled_ref.dtype)

    return _pcall(body, name=name, grid=(nb, POOL_G),
                  in_specs=[blk, pl.BlockSpec((1, LANE, LANE), lambda b, g: (g, 0, 0)),
                            pl.BlockSpec((1, LANE), lambda b, g: (0, g))],
                  out_specs=[blk, blk, blk],
                  out_shape=[_sds((T, POOL_W), MX), _sds((T, POOL_W), F32), _sds((T, POOL_W), MX)])(z, pool_grp, pool_scale)


def _pool_bwd(dscaled, mixed, pooled, pool_grp, pool_scale, *, nb, name):
    T = dscaled.shape[0]
    S = T // nb
    blk = pl.BlockSpec((S, LANE), lambda g, b: (b, g))

    def body(ds_ref, mixed_ref, pooled_ref, w_ref, s_ref, du_ref, dw_ref, dsc_ref):
        g = pl.program_id(0)
        b = pl.program_id(1)
        ds = ds_ref[...]
        dsc_ref[0] = jnp.sum(ds * mixed_ref[...], axis=0, keepdims=True)
        dmixed = (ds * s_ref[...]).astype(MX)
        dw = lax.dot_general(pooled_ref[...], dmixed, (((0,), (0,)), ((), ())), preferred_element_type=F32)

        @pl.when(b == 0)
        def _():
            dw_ref[0] = dw

        @pl.when(b > 0)
        def _():
            dw_ref[0] += dw
        dpooled = lax.dot_general(dmixed, w_ref[0], (((1,), (1,)), ((), ())), preferred_element_type=F32)
        row = lax.broadcasted_iota(jnp.int32, dpooled.shape, 0)
        width = lax.shift_left(jnp.int32(2), g)
        q = dpooled / jnp.minimum(row + 1, width).astype(F32)
        r2 = q + _shift_up(q, 1, row, S)
        r4 = r2 + _shift_up(r2, 2, row, S)
        r8 = r4 + _shift_up(r4, 4, row, S)
        r16 = r8 + _shift_up(r8, 8, row, S)
        win = jnp.where(g == 0, r2, jnp.where(g == 1, r4, jnp.where(g == 2, r8, r16)))
        du_ref[...] = (win - dpooled).astype(du_ref.dtype)

    return _pcall(body, name=name, grid=(POOL_G, nb),
                  in_specs=[blk, blk, blk, pl.BlockSpec((1, LANE, LANE), lambda g, b: (g, 0, 0)),
                            pl.BlockSpec((1, LANE), lambda g, b: (0, g))],
                  out_specs=[blk, pl.BlockSpec((1, LANE, LANE), lambda g, b: (g, 0, 0)),
                             pl.BlockSpec((1, 1, LANE), lambda g, b: (b, 0, g))],
                  out_shape=[_sds((T, POOL_W), MX), _sds((POOL_G, LANE, LANE), F32), _sds((nb, 1, POOL_W), F32)],
                  )(dscaled, mixed, pooled, pool_grp, pool_scale)


def _lane_masks(shape):
    lane = lax.broadcasted_iota(jnp.int32, shape, len(shape) - 1)
    m_n = lane < NOPE
    m_r = jnp.logical_and(lane >= KR_LANE, lane < KR_LANE + ROPE)
    first_half = lane < KR_LANE + ROPE // 2
    return m_n, m_r, first_half


def _rot(y, first_half):
    return jnp.where(first_half, -pltpu.roll(y, LANE - ROPE // 2, 1), pltpu.roll(y, ROPE // 2, 1))


def _rot_t(v, first_half, m_r):
    return jnp.where(m_r, jnp.where(first_half, pltpu.roll(v, LANE - ROPE // 2, 1), -pltpu.roll(v, ROPE // 2, 1)), 0.0)


def _mla_in_fwd(z, w_q, w_kv, qa_gain, kva_gain, cos, sin, q_gain, k_gain, *, name, tm=256):
    T = z.shape[0]
    slab = pl.BlockSpec((tm, LANE), lambda i: (i, 0))
    vec = pl.BlockSpec((1, LANE), lambda i: (0, 0))
    whole = lambda a: pl.BlockSpec(a.shape, lambda i: (0, 0))

    def body(z_ref, wq_ref, wkv_ref, qa_ref, kva_ref, cos_ref, sin_ref, qg_ref, kg_ref,
             qn_ref, kvn_ref, qp_ref, kvp_ref, q_ref, k_ref, v_ref):
        ql = z_ref[:, Z_QL:Z_QL + QL].astype(F32)
        kvl = z_ref[:, Z_KV:Z_KV + KVL].astype(F32)
        qn = (ql * _rsq(ql) * qa_ref[...]).astype(MX)
        kvn = (kvl * _rsq(kvl) * kva_ref[...]).astype(MX)
        qn_ref[...] = qn
        kvn_ref[...] = kvn
        qp = jnp.dot(qn, wq_ref[...], preferred_element_type=F32)
        kvp = jnp.dot(kvn, wkv_ref[...], preferred_element_type=F32)
        qp_ref[...] = qp
        kvp_ref[...] = kvp
        m_n, m_r, first_half = _lane_masks((tm, LANE))
        c = cos_ref[...]
        s = sin_ref[...]
        qg = qg_ref[...]
        kg = kg_ref[...]
        xr = z_ref[:, Z_KR:Z_KR + LANE].astype(F32)
        rr = lax.rsqrt(jnp.sum(xr * xr, axis=-1, keepdims=True) * (1.0 / ROPE) + EPS)
        yr = xr * rr * kg
        kr = jnp.where(m_r, yr * c + _rot(yr, first_half) * s, 0.0)
        for h in range(NH):
            x = qp[:, h * LANE:(h + 1) * LANE]
            x2 = x * x
            rn = lax.rsqrt(jnp.sum(jnp.where(m_n, x2, 0.0), axis=-1, keepdims=True) * (1.0 / NOPE) + EPS)
            rq = lax.rsqrt(jnp.sum(jnp.where(m_r, x2, 0.0), axis=-1, keepdims=True) * (1.0 / ROPE) + EPS)
            y = x * jnp.where(m_n, rn, jnp.where(m_r, rq, 0.0)) * qg
            q_ref[:, h * LANE:(h + 1) * LANE] = ((y * c + _rot(y, first_half) * s) * ATTN_SCALE).astype(q_ref.dtype)
            xk = kvp[:, h * LANE:(h + 1) * LANE]
            rk = lax.rsqrt(jnp.sum(jnp.where(m_n, xk * xk, 0.0), axis=-1, keepdims=True) * (1.0 / NOPE) + EPS)
            k_ref[:, h * LANE:(h + 1) * LANE] = (jnp.where(m_n, xk * rk * kg, 0.0) + kr).astype(k_ref.dtype)
        v_ref[...] = kvp[:, NH * LANE:].astype(v_ref.dtype)

    return _pcall(body, name=name, grid=(T // tm,),
                  in_specs=[_row_spec(tm, LAT_W), whole(w_q), whole(w_kv), whole(qa_gain), whole(kva_gain), slab, slab, vec, vec],
                  out_specs=[_row_spec(tm, QL), _row_spec(tm, KVL), _row_spec(tm, NH * LANE), _row_spec(tm, NH * LANE + NH * NOPE),
                             _row_spec(tm, NH * LANE), _row_spec(tm, NH * LANE), _row_spec(tm, NH * NOPE)],
                  out_shape=[_sds((T, QL), MX), _sds((T, KVL), MX), _sds((T, NH * LANE), F32), _sds((T, NH * LANE + NH * NOPE), F32),
                             _sds((T, NH * LANE), MX), _sds((T, NH * LANE), MX), _sds((T, NH * NOPE), MX)],
                  )(z, w_q, w_kv, qa_gain, kva_gain, cos, sin, q_gain, k_gain)


def _mla_in_bwd(dq, dk, dv, qp, kvp, z, w_q, w_kv, qa_gain, kva_gain, cos, sin, q_gain, k_gain, *, nb, name, tm=256):
    T = dq.shape[0]
    tps = (T // nb) // tm
    slab = pl.BlockSpec((tm, LANE), lambda i: (i, 0))
    vec = pl.BlockSpec((1, LANE), lambda i: (0, 0))
    whole = lambda a: pl.BlockSpec(a.shape, lambda i: (0, 0))

    def latent_bwd(x, dy, gain):
        r = _rsq(x)
        xhat = x * r
        dxhat = dy * gain
        return r * (dxhat - xhat * jnp.mean(dxhat * xhat, axis=-1, keepdims=True)), jnp.sum(dy * xhat, axis=0, keepdims=True)

    def body(dq_ref, dk_ref, dv_ref, qp_ref, kvp_ref, z_ref, wq_ref, wkv_ref, qa_ref, kva_ref, cos_ref, sin_ref, qg_ref, kg_ref,
             dqp_ref, dkvp_ref, dkr_ref, dql_ref, dkvl_ref, st_ref, sq_ref, sk_ref):
        i = pl.program_id(0)
        qp, kvp = qp_ref, kvp_ref
        m_n, m_r, first_half = _lane_masks((tm, LANE))
        c = cos_ref[...]
        s = sin_ref[...]
        qg = qg_ref[...]
        kg = kg_ref[...]
        dqg = jnp.zeros((1, LANE), F32)
        dkg = jnp.zeros((1, LANE), F32)
        dkr_sum = jnp.zeros((tm, LANE), F32)
        for h in range(NH):
            x = qp[:, h * LANE:(h + 1) * LANE]
            x2 = x * x
            rn = lax.rsqrt(jnp.sum(jnp.where(m_n, x2, 0.0), axis=-1, keepdims=True) * (1.0 / NOPE) + EPS)
            rq = lax.rsqrt(jnp.sum(jnp.where(m_r, x2, 0.0), axis=-1, keepdims=True) * (1.0 / ROPE) + EPS)
            rfac = jnp.where(m_n, rn, jnp.where(m_r, rq, 0.0))
            xhat = x * rfac
            do = dq_ref[:, h * LANE:(h + 1) * LANE] * ATTN_SCALE
            dy = do * c + _rot_t(do * s, first_half, m_r)
            dqg = dqg + jnp.sum(dy * xhat, axis=0, keepdims=True)
            dxhat = dy * qg
            t = dxhat * xhat
            mean_n = jnp.sum(jnp.where(m_n, t, 0.0), axis=-1, keepdims=True) * (1.0 / NOPE)
            mean_r = jnp.sum(jnp.where(m_r, t, 0.0), axis=-1, keepdims=True) * (1.0 / ROPE)
            dqp_ref[:, h * LANE:(h + 1) * LANE] = (
                rfac * (dxhat - xhat * jnp.where(m_n, mean_n, jnp.where(m_r, mean_r, 0.0)))).astype(dqp_ref.dtype)

            xk = kvp[:, h * LANE:(h + 1) * LANE]
            rk = lax.rsqrt(jnp.sum(jnp.where(m_n, xk * xk, 0.0), axis=-1, keepdims=True) * (1.0 / NOPE) + EPS)
            khat = jnp.where(m_n, xk * rk, 0.0)
            dko = dk_ref[:, h * LANE:(h + 1) * LANE]
            dkn = jnp.where(m_n, dko, 0.0)
            dkg = dkg + jnp.sum(dkn * khat, axis=0, keepdims=True)
            dkhat = dkn * kg
            mean_k = jnp.sum(dkhat * khat, axis=-1, keepdims=True) * (1.0 / NOPE)
            dkvp_ref[:, h * LANE:(h + 1) * LANE] = jnp.where(m_n, rk * (dkhat - khat * mean_k), 0.0).astype(dkvp_ref.dtype)
            dkr_sum = dkr_sum + jnp.where(m_r, dko, 0.0)
        dkvp_ref[:, NH * LANE:] = dv_ref[...].astype(dkvp_ref.dtype)

        xr = z_ref[:, Z_KR:Z_KR + LANE].astype(F32)
        rr = lax.rsqrt(jnp.sum(xr * xr, axis=-1, keepdims=True) * (1.0 / ROPE) + EPS)
        rhat = xr * rr
        dyr = dkr_sum * c + _rot_t(dkr_sum * s, first_half, m_r)
        dkg = dkg + jnp.sum(dyr * rhat, axis=0, keepdims=True)
        drhat = dyr * kg
        mean_kr = jnp.sum(drhat * rhat, axis=-1, keepdims=True) * (1.0 / ROPE)
        dkr_ref[...] = jnp.where(m_r, rr * (drhat - rhat * mean_kr), 0.0).astype(dkr_ref.dtype)

        nt = (((1,), (1,)), ((), ()))
        dqn = lax.dot_general(dqp_ref[...], wq_ref[...], nt, preferred_element_type=F32)
        dkvn = lax.dot_general(dkvp_ref[...], wkv_ref[...], nt, preferred_element_type=F32)
        dql, dqa = latent_bwd(z_ref[:, Z_QL:Z_QL + QL].astype(F32), dqn, qa_ref[...])
        dkvl, dkva = latent_bwd(z_ref[:, Z_KV:Z_KV + KVL].astype(F32), dkvn, kva_ref[...])
        dql_ref[...] = dql.astype(dql_ref.dtype)
        dkvl_ref[...] = dkvl.astype(dkvl_ref.dtype)

        @pl.when(i % tps == 0)
        def _():
            st_ref[...] = jnp.zeros_like(st_ref)
            sq_ref[...] = jnp.zeros_like(sq_ref)
            sk_ref[...] = jnp.zeros_like(sk_ref)

        st_ref[0, 0:1, :] += dqg
        st_ref[0, 1:2, :] += dkg
        sq_ref[0, 0:1, :] += dqa
        sk_ref[0, 0:1, :] += dkva

    stat = lambda w: pl.BlockSpec((1, SUBLANE, w), lambda i: (i // tps, 0, 0))
    return _pcall(body, name=name, grid=(T // tm,),
                  in_specs=[_row_spec(tm, NH * LANE), _row_spec(tm, NH * LANE), _row_spec(tm, NH * NOPE),
                            _row_spec(tm, NH * LANE), _row_spec(tm, NH * LANE + NH * NOPE), _row_spec(tm, LAT_W),
                            whole(w_q), whole(w_kv), whole(qa_gain), whole(kva_gain), slab, slab, vec, vec],
                  out_specs=[_row_spec(tm, NH * LANE), _row_spec(tm, NH * LANE + NH * NOPE), slab, _row_spec(tm, QL),
                             _row_spec(tm, KVL), stat(LANE), stat(QL), stat(KVL)],
                  out_shape=[_sds((T, NH * LANE), MX), _sds((T, NH * LANE + NH * NOPE), MX), _sds((T, LANE), MX),
                             _sds((T, QL), MX), _sds((T, KVL), MX), _sds((nb, SUBLANE, LANE), F32),
                             _sds((nb, SUBLANE, QL), F32), _sds((nb, SUBLANE, KVL), F32)],
                  )(dq, dk, dv, qp, kvp, z, w_q, w_kv, qa_gain, kva_gain, cos, sin, q_gain, k_gain)


def _lower_triangle(t, keys_first=False):
    key_axis = 0 if keys_first else 1
    return lax.broadcasted_iota(jnp.int32, (t, t), key_axis) <= lax.broadcasted_iota(jnp.int32, (t, t), 1 - key_axis)


def _attn_fwd(q, k, v, *, nb, name, tq=512, comm=None):
    T = q.shape[0]
    S = T // nb
    tq = _pick(S, tq)
    nq = S // tq
    tk = tq
    npair = NH // 2

    def body(q_ref, k_ref, v_ref, o_ref, lse_ref):
        qi = pl.program_id(2)
        lane = lax.broadcasted_iota(jnp.int32, (tq, LANE), 1)
        qs = [q_ref[:, hh * LANE:(hh + 1) * LANE] for hh in range(2)]

        def block(j, carry, diagonal):
            k0 = pl.multiple_of(j * tk, tk)
            vb = v_ref[pl.ds(k0, tk), :]
            vbs = [jnp.where(lane < NOPE, vb, jnp.ones_like(vb)), jnp.where(lane < NOPE, jnp.ones_like(vb), vb)]
            new = []
            for hh in range(2):
                m, acc = carry[hh]
                kb = k_ref[pl.ds(k0, tk), hh * LANE:(hh + 1) * LANE]
                s = lax.dot_general(qs[hh], kb, (((1,), (1,)), ((), ())), preferred_element_type=F32)
                if diagonal:
                    s = jnp.where(_lower_triangle(tq), s, NEG)
                m_new = jnp.maximum(m, jnp.max(s, axis=-1, keepdims=True))
                p = jnp.exp((s - m_new).astype(MX))
                acc = jnp.exp(m - m_new) * acc + jnp.dot(p, vbs[hh], preferred_element_type=F32)
                new.append((m_new, acc))
            return tuple(new)

        init = tuple((jnp.full((tq, 1), NEG, F32), jnp.zeros((tq, LANE), F32)) for _ in range(2))
        carry = lax.fori_loop(0, qi, lambda j, c: block(j, c, False), init)
        (m0, acc0), (m1, acc1) = block(qi, carry, True)
        l0, l1 = acc0[:, NOPE:NOPE + 1], acc1[:, 0:1]
        o_ref[...] = jnp.where(lane < NOPE, acc0 / l0, acc1 / l1).astype(o_ref.dtype)
        lse_ref[...] = jnp.where(lane < NOPE, m0 + jnp.log(l0), m1 + jnp.log(l1))

    return _pcall(body, name=name, grid=(nb, npair, nq),
                  in_specs=[pl.BlockSpec((tq, 2 * LANE), lambda b, p, i: (b * nq + i, p)),
                            pl.BlockSpec((S, 2 * LANE), lambda b, p, i: (b, p)),
                            pl.BlockSpec((S, LANE), lambda b, p, i: (b, p))],
                  out_specs=[pl.BlockSpec((tq, LANE), lambda b, p, i: (b * nq + i, p)),
                             pl.BlockSpec((tq, LANE), lambda b, p, i: (b * nq + i, p))],
                  out_shape=[_sds((T, NH * NOPE), MX), _sds((T, NH * NOPE), F32)], comm=comm)(q, k, v)


def _attn_bwd(q, k, v, o, lse, do, *, nb, name, tq=512, comm=None):
    T = q.shape[0]
    S = T // nb
    tq = _pick(S, tq)
    nq = S // tq
    tk = tq
    npair = NH // 2

    def body(q_ref, k_ref, v_ref, o_ref, lse_ref, do_ref, dq_ref, dk_ref, dv_ref, lse_t, delta_t):
        lane = lax.broadcasted_iota(jnp.int32, (tq, LANE), 1)
        first = lane < NOPE
        dq_ref[...] = jnp.zeros_like(dq_ref)

        def rows_step(qi, _):
            q0 = pl.multiple_of(qi * tq, tq)
            prod = do_ref[pl.ds(q0, tq), :] * o_ref[pl.ds(q0, tq), :].astype(F32)
            d0 = jnp.sum(jnp.where(first, prod, 0.0), axis=-1, keepdims=True)
            d1 = jnp.sum(jnp.where(first, 0.0, prod), axis=-1, keepdims=True)
            delta_t[:, pl.ds(q0, tq)] = jnp.where(first, d0, d1).T
            lse_t[:, pl.ds(q0, tq)] = lse_ref[pl.ds(q0, tq), :].T
            return 0

        lax.fori_loop(0, nq, rows_step, 0)

        def kv_step(kj, _):
            k0 = pl.multiple_of(kj * tk, tk)
            kbs = [k_ref[pl.ds(k0, tk), hh * LANE:(hh + 1) * LANE] for hh in range(2)]
            vb = v_ref[pl.ds(k0, tk), :]

            def q_block(qi, carry, diagonal):
                dk0, dk1, dv = carry
                dks = [dk0, dk1]
                q0 = pl.multiple_of(qi * tq, tq)
                dov = do_ref[pl.ds(q0, tq), :]
                for hh in range(2):
                    qh = q_ref[pl.ds(q0, tq), hh * LANE:(hh + 1) * LANE]
                    dob = jnp.where(first if hh == 0 else jnp.logical_not(first), dov, 0.0).astype(MX)
                    st = lax.dot_general(kbs[hh], qh, (((1,), (1,)), ((), ())), preferred_element_type=F32)
                    pt = jnp.exp((st - lse_t[hh * NOPE:hh * NOPE + 1, pl.ds(q0, tq)]).astype(MX))
                    if diagonal:
                        pt = jnp.where(_lower_triangle(tq, keys_first=True), pt, jnp.zeros_like(pt))
                    dpt = lax.dot_general(vb, dob, (((1,), (1,)), ((), ())), preferred_element_type=F32)
                    dst = pt * (dpt - delta_t[hh * NOPE:hh * NOPE + 1, pl.ds(q0, tq)]).astype(MX)
                    dks[hh] = dks[hh] + jnp.dot(dst, qh, preferred_element_type=F32)
                    dv = dv + jnp.dot(pt, dob, preferred_element_type=F32)
                    dq_ref[pl.ds(q0, tq), hh * LANE:(hh + 1) * LANE] += lax.dot_general(
                        dst, kbs[hh], (((0,), (0,)), ((), ())), preferred_element_type=F32)
                return dks[0], dks[1], dv

            zero = jnp.zeros((tk, LANE), F32)
            carry = q_block(kj, (zero, zero, zero), True)
            dk0, dk1, dv = lax.fori_loop(kj + 1, nq, lambda qi, c: q_block(qi, c, False), carry)
            dk_ref[pl.ds(k0, tk), 0:LANE] = dk0
            dk_ref[pl.ds(k0, tk), LANE:2 * LANE] = dk1
            dv_ref[pl.ds(k0, tk), :] = dv
            return 0

        lax.fori_loop(0, nq, kv_step, 0)

    pair256 = pl.BlockSpec((S, 2 * LANE), lambda b, p: (b, p))
    pair128 = pl.BlockSpec((S, LANE), lambda b, p: (b, p))
    return _pcall(body, name=name, grid=(nb, npair),
                  in_specs=[pair256, pair256, pair128, pair128, pair128, pair128],
                  out_specs=[pair256, pair256, pair128],
                  out_shape=[_sds((T, NH * LANE), F32), _sds((T, NH * LANE), F32), _sds((T, NH * NOPE), F32)],
                  scratch=[pltpu.VMEM((LANE, S), F32), pltpu.VMEM((LANE, S), F32)], comm=comm)(q, k, v, o, lse, do)


def _run(plan, fn, *args, name, **kw):
    rider = plan.rider(name)
    if rider is None:
        return fn(*args, name=name, **kw)
    outs, landed = fn(*args, name=name, comm=rider, **kw)
    plan.landed(name, landed)
    return outs


def _layer_fwd_bwd(x, tgt, mod3, cos, sin, plan, P):
    nb = mod3.shape[0]
    W = plan.W
    h1, gu1, a1 = _run(plan, _ffn_up_first, x, mod3, P["norm_ffn1"], W["ffn1_in"], sub=0, name="ffn1_up_a")
    gu1, a1 = _run(plan, _ffn_up_second, h1, W["ffn1_in"], gu1, a1, name="ffn1_up_b")
    f1, x1, h2 = _run(plan, _mm_resid_norm, a1, W["ffn1_out"], x, mod3, P["norm_mix"], sub=1, coef=0.5, name="ffn1_out")
    z = _run(plan, _mm, h2, W["w_in"], mode="nt", out_dtype=MX, name="mix_in", tn=1664)
    pooled, mixed, scaled = _pool_fwd(z, W["pool_grp"], P["pool_scale"], nb=nb, name="pool_fwd")
    br_pool = _mm(scaled, W["pool_proj"], mode="nn", out_dtype=MX, name="pool_proj")
    qn, kvn, qp, kvp, q, k, v = _mla_in_fwd(z, W["q_up"], W["kv_up"], P["q_a_norm"], P["kv_a_norm"], cos, sin,
                                            P["q_gain"], P["k_gain"], name="mla_in_fwd")
    attn, lse = _run(plan, _attn_fwd, q, k, v, nb=nb, name="attn_fwd")
    br_mla, merged, mo, x2, h3 = _mix_out_fwd(z, br_pool, attn, W["mla_proj"], W["w_out"], x1, mod3, P["norm_ffn2"], sub=2,
                                              name="mix_out", tm=256)
    gu2, a2 = _ffn_up(h3, W["ffn2_in"], name="ffn2_up")
    dy, df2, st_fin, loss = _mm_loss(a2, W["ffn2_out"], x2, tgt, mod3, name="ffn2_out_loss")

    plan.grad("ffn2_out", _dw(a2, df2, name="d_ffn2_out"))
    dgu2 = _ffn_dact(df2, gu2, W["ffn2_out"], name="ffn2_dact")
    plan.grad("ffn2_in", _ffn_dw_in(h3, dgu2, name="d_ffn2_in"))
    dx2, dmo, st3 = _run(plan, _ffn_dh_norm, dgu2, W["ffn2_in"], x2, dy, mo, mod3, P["norm_ffn2"], sub=2, coef=1.0,
                         name="d_ffn2_h")
    plan.grad("w_out", _dw(merged, dmo, name="d_mix_out"))
    dbr_pool, dbr_mla, dgates, dattn = _mix_out_bwd(dmo, W["w_out"], z, br_pool, br_mla, W["mla_proj"], name="mix_out_bwd", tm=256)
    plan.grad("pool_proj", _dw(scaled, dbr_pool, name="d_pool_proj"))
    dscaled = _mm(dbr_pool, W["pool_proj"], mode="nt", out_dtype=F32, name="d_pool_scaled")
    du_pool, d_pool_grp, d_pool_scale = _pool_bwd(dscaled, mixed, pooled, W["pool_grp"], P["pool_scale"], nb=nb, name="pool_bwd")
    plan.grad("mla_proj", _dw(attn, dbr_mla, name="d_mla_proj"))
    dq, dk, dv = _run(plan, _attn_bwd, q, k, v, attn, lse, dattn, nb=nb, name="attn_bwd")
    dqp, dkvp, dkr, dql, dkvl, st_prep, st_q, st_kv = _mla_in_bwd(
        dq, dk, dv, qp, kvp, z, W["q_up"], W["kv_up"], P["q_a_norm"], P["kv_a_norm"], cos, sin, P["q_gain"], P["k_gain"],
        nb=nb, name="mla_in_bwd")
    plan.grad("q_up", _dw(qn, dqp, name="d_q_up"))
    plan.grad("kv_up", _dw(kvn, dkvp, name="d_kv_up"))
    dz = jnp.concatenate([du_pool, dql, dkvl, dkr, dgates], axis=1)
    plan.grad("w_in", _run(plan, _dw, h2, dz, name="d_mix_in", tm=256, tn=1664, out_t=True))
    dx1, df1, st2 = _run(plan, _mm_norm_bwd, dz, W["w_in"], x1, dx2, f1, mod3, P["norm_mix"], sub=1, coef=0.5, name="d_mix_h")
    plan.grad("ffn1_out", _run(plan, _dw, a1, df1, name="d_ffn1_out"))
    dgu1 = _run(plan, _ffn_dact, df1, gu1, W["ffn1_out"], name="ffn1_dact")
    half = x.shape[1] // 2
    plan.grad("ffn1_in@0", _run(plan, _ffn_dw_in, h1, dgu1, rows=(0, half), name="d_ffn1_in_a"))
    plan.grad("ffn1_in@1", _run(plan, _ffn_dw_in, h1, dgu1, rows=(half, half), name="d_ffn1_in_b"))
    dh1 = _run(plan, _ffn_dh, dgu1, W["ffn1_in"], name="d_ffn1_h")
    grad_x, st1 = _run(plan, _bwd_block, x, dh1, dx1, None, mod3, P["norm_ffn1"], sub=0, coef=0.0, name="bwd_norm1")

    return loss, grad_x, (st1, st2, st3, st_fin, st_q, st_kv, st_prep, d_pool_scale), d_pool_grp


def _padded(rows):
    return rows + -rows % (4 * SUBLANE)


def _pad_rows(a):
    pad = [(0, 0)] * a.ndim
    pad[-2] = (0, _padded(a.shape[-2]) - a.shape[-2])
    return jnp.pad(a, pad)


def _w_in_pieces(shard):
    runs = []
    for ref0, ref1, k0 in ((0, Z_KR, 0), (Z_KR, Z_KR + ROPE, Z_KR + KR_LANE), (Z_KR + ROPE, N_CHIP * shard, Z_GP)):
        for q in range(N_CHIP):
            lo, hi = max(ref0, q * shard), min(ref1, (q + 1) * shard)
            if lo < hi:
                runs.append((k0 + lo - ref0, q * _padded(shard) + lo - q * shard, hi - lo))
    return runs


def _w_in_stacked_to_kernel(st, shard):
    flat = st.reshape(-1, st.shape[2])
    parts, at = [], 0
    for k0, s0, n in sorted(_w_in_pieces(shard)):
        parts += [jnp.zeros((k0 - at, flat.shape[1]), st.dtype)] * (k0 > at) + [flat[s0:s0 + n]]
        at = k0 + n
    assert at == Z_W
    return jnp.concatenate(parts, axis=0)


def _w_in_kernel_to_stacked(gt, shard):
    parts, at = [], 0
    for s0, k0, n in sorted((s, k, n) for k, s, n in _w_in_pieces(shard)) + [(N_CHIP * _padded(shard), 0, 0)]:
        parts += [jnp.zeros((s0 - at, gt.shape[1]), gt.dtype)] * (s0 > at) + [gt[k0:k0 + n]] * (n > 0)
        at = s0 + n
    return jnp.concatenate(parts, axis=0).reshape(N_CHIP, _padded(shard), gt.shape[1])


def _q_up_to_kernel(w):
    k = w.shape[0]
    return jnp.pad(w.reshape(k, NH, NOPE + ROPE), ((0, 0), (0, 0), (0, LANE - NOPE - ROPE))).reshape(k, NH * LANE)


def _q_up_from_kernel(g):
    k = g.shape[0]
    return g.reshape(k, NH, LANE)[:, :, :NOPE + ROPE].reshape(k, NH * (NOPE + ROPE))


def _kv_up_to_kernel(w):
    k = w.shape[0]
    w3 = w.reshape(k, NH, 2 * NOPE)
    kpart = jnp.pad(w3[:, :, :NOPE], ((0, 0), (0, 0), (0, LANE - NOPE))).reshape(k, NH * LANE)
    return jnp.concatenate([kpart, w3[:, :, NOPE:].reshape(k, NH * NOPE)], axis=1)


def _kv_up_from_kernel(g):
    k = g.shape[0]
    kpart = g[:, :NH * LANE].reshape(k, NH, LANE)[:, :, :NOPE]
    vpart = g[:, NH * LANE:].reshape(k, NH, NOPE)
    return jnp.concatenate([kpart, vpart], axis=2).reshape(k, NH * 2 * NOPE)


def _gain_slab(nope, rope):
    return jnp.concatenate([nope, rope, jnp.zeros((1, LANE - NOPE - ROPE), nope.dtype)], axis=1)


def _rope_tables(positions):
    inv_freq = 10000.0 ** (-jnp.arange(0, ROPE, 2, dtype=F32) / ROPE)
    ang = positions.astype(F32)[:, None] * inv_freq
    ang = jnp.concatenate([ang, ang], axis=-1)
    t = positions.shape[0]
    cos = jnp.concatenate([jnp.ones((t, KR_LANE), F32), jnp.cos(ang), jnp.ones((t, LANE - KR_LANE - ROPE), F32)], axis=1)
    sin = jnp.concatenate([jnp.zeros((t, KR_LANE), F32), jnp.sin(ang), jnp.zeros((t, LANE - KR_LANE - ROPE), F32)], axis=1)
    return cos, sin


def _coords():
    return lax.axis_index("x"), lax.axis_index("y"), lax.axis_index("c")


HBM_SPEC = pl.BlockSpec(memory_space=pl.ANY)
VMEM_SPEC = pl.BlockSpec(memory_space=pltpu.VMEM)


CHIP_RELS = ((1, 0), (0, 1), (1, 1))


def _gather_comm(shards):
    n = len(shards)

    def ici(ins, outs, sems, a, j, x, y, cc):
        half = ins[a].shape[0] // 2
        mine = pl.ds(cc * half, half)
        dx, dy = CHIP_RELS[j]
        return pltpu.make_async_remote_copy(src_ref=ins[a].at[mine], dst_ref=outs[a].at[2 * x + y, mine],
                                            send_sem=sems[0].at[a, j], recv_sem=sems[1].at[a, j],
                                            device_id=(x ^ dx, y ^ dy, cc), device_id_type=MESH)

    def d2d(ins, outs, sems, a, j, x, y, cc, half_of):
        half = ins[a].shape[0] // 2
        dx, dy = CHIP_RELS[j]
        landed = outs[a].at[2 * (x ^ dx) + (y ^ dy), pl.ds(half_of * half, half)]
        return pltpu.make_async_remote_copy(src_ref=landed, dst_ref=landed, send_sem=sems[2].at[a, j], recv_sem=sems[3].at[a, j],
                                            device_id=(x, y, 1 - cc), device_id_type=MESH)

    def own(ins, outs, sems, a, x, y):
        return pltpu.make_async_copy(ins[a], outs[a].at[2 * x + y], sems[4].at[a])

    def start(ins, outs, sems):
        x, y, cc = _coords()
        for a in range(n):
            own(ins, outs, sems, a, x, y).start()
            for j in range(3):
                ici(ins, outs, sems, a, j, x, y, cc).start()

    def finish(ins, outs, sems):
        x, y, cc = _coords()
        for a in range(n):
            for j in range(3):
                ici(ins, outs, sems, a, j, x, y, cc).wait_recv()
                d2d(ins, outs, sems, a, j, x, y, cc, cc).start()
        for a in range(n):
            for j in range(3):
                d2d(ins, outs, sems, a, j, x, y, cc, 1 - cc).wait_recv()
        for a in range(n):
            for j in range(3):
                ici(ins, outs, sems, a, j, x, y, cc).wait_send()
                d2d(ins, outs, sems, a, j, x, y, cc, cc).wait_send()
            own(ins, outs, sems, a, x, y).wait()

    dma = pltpu.SemaphoreType.DMA
    return _Comm(shards, [_sds((N_CHIP,) + s.shape, s.dtype) for s in shards],
                 [dma((n, 3)), dma((n, 3)), dma((n, 3)), dma((n, 3)), dma((n,))], start, finish)


def _swap_comm(parts):
    n = len(parts)

    def copy(ins, outs, sems, a):
        x, y, cc = _coords()
        half = ins[a].shape[1] // 2
        return pltpu.make_async_remote_copy(src_ref=ins[a].at[:, pl.ds((1 - cc) * half, half)], dst_ref=outs[a],
                                            send_sem=sems[0].at[a], recv_sem=sems[1].at[a], device_id=(x, y, 1 - cc),
                                            device_id_type=MESH)

    def start(ins, outs, sems):
        for a in range(n):
            copy(ins, outs, sems, a).start()

    def finish(ins, outs, sems):
        for a in range(n):
            copy(ins, outs, sems, a).wait()

    dma = pltpu.SemaphoreType.DMA
    return _Comm(parts, [_sds((p.shape[0], p.shape[1] // 2, p.shape[2]), p.dtype) for p in parts], [dma((n,)), dma((n,))],
                 start, finish)


def _add_half(full, other, cidx, *, name):
    nch, r, c = full.shape
    half = r // 2
    tr = _pick_rows(half)
    nbk = half // tr
    grid_spec = pltpu.PrefetchScalarGridSpec(
        num_scalar_prefetch=1, grid=(nch, nbk),
        in_specs=[pl.BlockSpec((1, tr, c), lambda j, i, cref: (j, cref[0] * nbk + i, 0)),
                  pl.BlockSpec((1, tr, c), lambda j, i, cref: (j, i, 0))],
        out_specs=pl.BlockSpec((1, tr, c), lambda j, i, cref: (j, i, 0)))

    def body(cref, a_ref, b_ref, o_ref):
        o_ref[...] = (a_ref[...] + b_ref[...]).astype(o_ref.dtype)

    return _pcall(body, name=name, out_shape=_sds((nch, half, c), MX), grid_spec=grid_spec)(cidx, full, other)


def _pick_rows(rows, target=512):
    best = None
    for t in range(16, min(rows, target) + 1, 16):
        if rows % t == 0:
            best = t
    return rows if best is None else best


def _exchange_comm(parts):
    n = len(parts)

    def send(ins, outs, sems, a, j, x, y, cc):
        dx, dy = CHIP_RELS[j]
        return pltpu.make_async_remote_copy(src_ref=ins[a].at[2 * (x ^ dx) + (y ^ dy)], dst_ref=outs[a].at[2 * x + y],
                                            send_sem=sems[0].at[a, j], recv_sem=sems[1].at[a, j],
                                            device_id=(x ^ dx, y ^ dy, cc), device_id_type=MESH)

    def landing(ins, outs, sems, a, j, x, y, cc):
        dx, dy = CHIP_RELS[j]
        peer_chip = 2 * (x ^ dx) + (y ^ dy)
        return pltpu.make_async_remote_copy(src_ref=ins[a].at[peer_chip], dst_ref=outs[a].at[peer_chip], send_sem=sems[0].at[a, j],
                                            recv_sem=sems[1].at[a, j], device_id=(x, y, cc), device_id_type=MESH)

    def own(ins, outs, sems, a, x, y):
        return pltpu.make_async_copy(ins[a].at[2 * x + y], outs[a].at[2 * x + y], sems[2].at[a])

    def start(ins, outs, sems):
        x, y, cc = _coords()
        for a in range(n):
            own(ins, outs, sems, a, x, y).start()
            for j in range(3):
                send(ins, outs, sems, a, j, x, y, cc).start()

    def finish(ins, outs, sems):
        x, y, cc = _coords()
        for a in range(n):
            for j in range(3):
                landing(ins, outs, sems, a, j, x, y, cc).wait_recv()
        for a in range(n):
            for j in range(3):
                send(ins, outs, sems, a, j, x, y, cc).wait_send()
            own(ins, outs, sems, a, x, y).wait()

    dma = pltpu.SemaphoreType.DMA
    return _Comm(parts, [_sds(p.shape, p.dtype) for p in parts], [dma((n, 3)), dma((n, 3)), dma((n,))], start, finish)


def _sum_chips(q, cidx, *, name):
    nch, h, c = q.shape
    tr = _pick_rows(h)
    nbk = h // tr
    grid_spec = pltpu.PrefetchScalarGridSpec(
        num_scalar_prefetch=1, grid=(nbk,),
        in_specs=[pl.BlockSpec((nch, tr, c), lambda i, cref: (0, i, 0))],
        out_specs=pl.BlockSpec((tr, c), lambda i, cref: (cref[0] * nbk + i, 0)))

    def body(cref, q_ref, o_ref):
        acc = q_ref[0].astype(F32) + q_ref[1].astype(F32)
        acc = acc + q_ref[2].astype(F32)
        o_ref[...] = acc + q_ref[3].astype(F32)

    return _pcall(body, name=name, out_shape=_sds((2 * h, c), F32), grid_spec=grid_spec)(cidx, q)


def _join_comm(fulls):
    n = len(fulls)

    def half_copy(outs, sems, a, which):
        x, y, cc = _coords()
        h = outs[a].shape[0] // 2
        rows = outs[a].at[pl.ds((cc if which == 0 else 1 - cc) * h, h)]
        return pltpu.make_async_remote_copy(src_ref=rows, dst_ref=rows, send_sem=sems[0].at[a], recv_sem=sems[1].at[a],
                                            device_id=(x, y, 1 - cc), device_id_type=MESH)

    def start(ins, outs, sems):
        for a in range(n):
            half_copy(outs, sems, a, 0).start()

    def finish(ins, outs, sems):
        for a in range(n):
            half_copy(outs, sems, a, 1).wait_recv()
        for a in range(n):
            half_copy(outs, sems, a, 0).wait_send()

    dma = pltpu.SemaphoreType.DMA
    return _Comm(fulls, [_sds(p.shape, p.dtype) for p in fulls], [dma((n,)), dma((n,))], start, finish,
                 aliases={a: a for a in range(n)})


def _ada_prologue(c, w, b, to_cast, *, name, comm=None):
    nb, dm = c.shape
    n = w.shape[1]

    nc = len(to_cast)

    def body(c_ref, w_ref, b_ref, *rest):
        src, (call_ref, land_ref), dst = rest[:nc], rest[nc:nc + 2], rest[nc + 2:2 * nc + 2]
        cpad, mod_s, g_send, g_recv, m_send, m_recv = rest[2 * nc + 2:2 * nc + 8]
        wide, narrow, in_sem, out_sem = rest[2 * nc + 8:3 * nc + 8], rest[3 * nc + 8:4 * nc + 8], rest[-2], rest[-1]
        loads = [pltpu.make_async_copy(src[i], wide[i], in_sem.at[i]) for i in range(nc)]
        stores = [pltpu.make_async_copy(narrow[i], dst[i], out_sem.at[i]) for i in range(nc)]
        loads[0].start()
        x, y, cc = _coords()
        me = 4 * x + 2 * y + cc
        chip = 2 * x + y
        cpad[...] = jnp.zeros_like(cpad)
        cpad[0:nb, :] = c_ref[...]
        copies = []
        for kk in range(1, N_DEV):
            peer = (x ^ (kk >> 2), y ^ ((kk >> 1) & 1), cc ^ (kk & 1))
            cp = pltpu.make_async_remote_copy(src_ref=cpad, dst_ref=call_ref.at[me], send_sem=g_send.at[kk - 1],
                                              recv_sem=g_recv.at[kk - 1], device_id=peer, device_id_type=MESH)
            cp.start()
            copies.append(cp)
        call_ref[me] = cpad[...]
        for kk in range(1, N_DEV):
            pltpu.make_async_remote_copy(src_ref=cpad, dst_ref=call_ref.at[me ^ kk], send_sem=g_send.at[kk - 1],
                                         recv_sem=g_recv.at[kk - 1], device_id=(x, y, cc), device_id_type=MESH).wait_recv()
        cv = call_ref[...].reshape(N_DEV * SUBLANE, dm)
        act = (cv * _sigmoid(cv)).astype(MX)
        mod = jnp.dot(act, w_ref[...].astype(MX), preferred_element_type=F32) + b_ref[...]
        mod_s[...] = mod.reshape(N_DEV, SUBLANE, n)
        for j, (dx, dy) in enumerate(CHIP_RELS):
            cp = pltpu.make_async_remote_copy(src_ref=mod_s.at[4 * (x ^ dx) + 2 * (y ^ dy) + cc], dst_ref=land_ref.at[chip],
                                              send_sem=m_send.at[j], recv_sem=m_recv.at[j],
                                              device_id=(x ^ dx, y ^ dy, cc), device_id_type=MESH)
            cp.start()
            copies.append(cp)
        land_ref[chip] = mod_s[me]
        for i in range(nc):
            loads[i].wait()
            if i + 1 < nc:
                loads[i + 1].start()
            narrow[i][...] = wide[i][...].astype(MX)
            stores[i].start()
        for j, (dx, dy) in enumerate(CHIP_RELS):
            pltpu.make_async_remote_copy(src_ref=mod_s.at[me], dst_ref=land_ref.at[2 * (x ^ dx) + (y ^ dy)], send_sem=m_send.at[j],
                                         recv_sem=m_recv.at[j], device_id=(x, y, cc), device_id_type=MESH).wait_recv()
        for cp in copies:
            cp.wait_send()
        for st in stores:
            st.wait()

    dma = pltpu.SemaphoreType.DMA
    return _pcall(body, name=name, in_specs=[VMEM_SPEC] * 3 + [HBM_SPEC] * nc, out_specs=[VMEM_SPEC] * 2 + [HBM_SPEC] * nc,
                  out_shape=[_sds((N_DEV, SUBLANE, dm), F32), _sds((N_CHIP, SUBLANE, n), F32)]
                  + [_sds(a.shape, MX) for a in to_cast],
                  scratch=[pltpu.VMEM((SUBLANE, dm), F32), pltpu.VMEM((N_DEV, SUBLANE, n), F32),
                           dma((N_DEV - 1,)), dma((N_DEV - 1,)), dma((3,)), dma((3,))]
                  + [pltpu.VMEM(a.shape, a.dtype) for a in to_cast] + [pltpu.VMEM(a.shape, MX) for a in to_cast]
                  + [dma((nc,)), dma((nc,))], comm=comm)(c, w, b, *to_cast)


def _ada_bwd(c_all, dmod_cols, *, name):
    m, kdim = c_all.shape
    n = dmod_cols.shape[1]
    tn = _pick(n, 1152)

    def body(c_ref, d_ref, o_ref):
        cv = c_ref[...]
        act = (cv * _sigmoid(cv)).astype(MX)
        o_ref[...] = lax.dot_general(act, d_ref[...].astype(MX), (((0,), (0,)), ((), ())), preferred_element_type=F32)

    return _pcall(body, name=name, out_shape=_sds((kdim, n), F32), grid=(n // tn,),
                  in_specs=[pl.BlockSpec((m, kdim), lambda j: (0, 0)), pl.BlockSpec((m, tn), lambda j: (0, j))],
                  out_specs=pl.BlockSpec((kdim, tn), lambda j: (0, j)))(c_all, dmod_cols)


SLAB_W = 1024
RED_ROWS = 8


LOSS_LANE = SLAB_W - LANE


def _pack_stats(st1, st2, st3, st_fin, st_q, st_kv, st_prep, d_pool_scale, loss8, *, name):
    nb = st1.shape[0]
    dm_rows = -(-9 * nb // SUBLANE) * SUBLANE

    def body(s1, s2, s3, sf, sq, skv, sp, sps, loss_ref, o_ref):
        o_ref[...] = jnp.zeros_like(o_ref)
        for s in range(nb):
            rows = [s1[s, 0:1, :], s1[s, 1:2, :], s2[s, 3:4, :], s2[s, 0:1, :], s2[s, 1:2, :], s3[s, 3:4, :], s3[s, 0:1, :],
                    s3[s, 1:2, :], sf[s, 0:1, :]]
            for k, row in enumerate(rows):
                o_ref[9 * s + k:9 * s + k + 1, :] = row

        def over_seq(ref, r):
            acc = ref[0, r:r + 1, :]
            for s in range(1, nb):
                acc = acc + ref[s, r:r + 1, :]
            return acc

        o_ref[dm_rows + 0:dm_rows + 1, :] = over_seq(s1, 2)
        o_ref[dm_rows + 1:dm_rows + 2, :] = over_seq(s2, 2)
        o_ref[dm_rows + 2:dm_rows + 3, :] = over_seq(s3, 2)
        o_ref[dm_rows + 3:dm_rows + 4, 0:POOL_W] = over_seq(sps, 0)
        o_ref[dm_rows + 4:dm_rows + 5, 0:QL] = over_seq(sq, 0)
        o_ref[dm_rows + 5:dm_rows + 6, 0:KVL] = over_seq(skv, 0)
        o_ref[dm_rows + 6:dm_rows + 7, 0:LANE] = over_seq(sp, 0)
        o_ref[dm_rows + 7:dm_rows + 8, 0:LANE] = over_seq(sp, 1)
        o_ref[dm_rows + 7:dm_rows + 8, LOSS_LANE:] = loss_ref[0:1, :]

    return _pcall(body, name=name, out_shape=_sds((dm_rows + RED_ROWS, SLAB_W), F32), in_specs=[VMEM_SPEC] * 9,
                  out_specs=VMEM_SPEC)(st1, st2, st3, st_fin, st_q, st_kv, st_prep, d_pool_scale, loss8)


def _small_allreduce(slab, pool, *, name, comm=None):
    rows, w = slab.shape
    dm = rows - RED_ROWS
    prow, pw = pool.shape
    crow = RED_ROWS + 2 * dm

    def body(slab_ref, pool_ref, red_ref, dmod_ref, ptot_ref, sib_slab, sib_pool, chip_slab, chip_pool, land_slab, land_pool,
             a_send, a_recv, b_send, b_recv):
        x, y, cc = _coords()
        chip = 2 * x + y
        sib = (x, y, 1 - cc)
        to_sib = [pltpu.make_async_remote_copy(src_ref=slab_ref, dst_ref=sib_slab, send_sem=a_send.at[0], recv_sem=a_recv.at[0],
                                               device_id=sib, device_id_type=MESH),
                  pltpu.make_async_remote_copy(src_ref=pool_ref, dst_ref=sib_pool, send_sem=a_send.at[1], recv_sem=a_recv.at[1],
                                               device_id=sib, device_id_type=MESH)]
        for cp in to_sib:
            cp.start()
        for cp in to_sib:
            cp.wait()
        mine_dm, theirs_dm = slab_ref[0:dm, :], sib_slab[0:dm, :]
        chip_slab[0:RED_ROWS, :] = slab_ref[dm:, :] + sib_slab[dm:, :]
        chip_slab[RED_ROWS:RED_ROWS + dm, :] = jnp.where(cc == 0, mine_dm, theirs_dm)
        chip_slab[RED_ROWS + dm:, :] = jnp.where(cc == 0, theirs_dm, mine_dm)
        chip_pool[...] = pool_ref[...] + sib_pool[...]

        sends = []
        for j, (dx, dy) in enumerate(CHIP_RELS):
            peer = (x ^ dx, y ^ dy, cc)
            sends.append(pltpu.make_async_remote_copy(src_ref=chip_slab, dst_ref=land_slab.at[chip], send_sem=b_send.at[j, 0],
                                                      recv_sem=b_recv.at[j, 0], device_id=peer, device_id_type=MESH))
            sends.append(pltpu.make_async_remote_copy(src_ref=chip_pool, dst_ref=land_pool.at[chip], send_sem=b_send.at[j, 1],
                                                      recv_sem=b_recv.at[j, 1], device_id=peer, device_id_type=MESH))
        for cp in sends:
            cp.start()
        land_slab[chip] = chip_slab[...]
        land_pool[chip] = chip_pool[...]
        for j, (dx, dy) in enumerate(CHIP_RELS):
            peer_chip = 2 * (x ^ dx) + (y ^ dy)
            pltpu.make_async_remote_copy(src_ref=chip_slab, dst_ref=land_slab.at[peer_chip], send_sem=b_send.at[j, 0],
                                         recv_sem=b_recv.at[j, 0], device_id=(x, y, cc), device_id_type=MESH).wait_recv()
            pltpu.make_async_remote_copy(src_ref=chip_pool, dst_ref=land_pool.at[peer_chip], send_sem=b_send.at[j, 1],
                                         recv_sem=b_recv.at[j, 1], device_id=(x, y, cc), device_id_type=MESH).wait_recv()
        red = land_slab[0, 0:RED_ROWS, :]
        ptot = land_pool[0]
        for ch in range(1, N_CHIP):
            red = red + land_slab[ch, 0:RED_ROWS, :]
            ptot = ptot + land_pool[ch]
        red_ref[...] = red
        ptot_ref[...] = ptot
        for ch in range(N_CHIP):
            dmod_ref[2 * dm * ch:2 * dm * (ch + 1), :] = land_slab[ch, RED_ROWS:, :]
        for cp in sends:
            cp.wait_send()

    dma = pltpu.SemaphoreType.DMA
    return _pcall(body, name=name, in_specs=[VMEM_SPEC, VMEM_SPEC], out_specs=[VMEM_SPEC] * 3,
                  out_shape=[_sds((RED_ROWS, w), F32), _sds((N_DEV * dm, w), F32), _sds((prow, pw), F32)],
                  scratch=[pltpu.VMEM((rows, w), F32), pltpu.VMEM((prow, pw), F32), pltpu.VMEM((crow, w), F32),
                           pltpu.VMEM((prow, pw), F32), pltpu.VMEM((N_CHIP, crow, w), F32), pltpu.VMEM((N_CHIP, prow, pw), F32),
                           dma((2,)), dma((2,)), dma((3, 2)), dma((3, 2))], comm=comm)(slab, pool)


def _adamw_math(w, g, m, v):
    mn = ADAM_B1 * m + (1.0 - ADAM_B1) * g
    vn = ADAM_B2 * v + (1.0 - ADAM_B2) * (g * g)
    bc1 = 1.0 / (1.0 - ADAM_B1 ** ADAM_STEP)
    bc2 = 1.0 / (1.0 - ADAM_B2 ** ADAM_STEP)
    return -ADAM_LR * ((mn * bc1) / (jnp.sqrt(vn * bc2) + ADAM_EPS) + ADAM_WD * w), mn, vn


def _small_update(red, dmod_all, pool_total, nb, params, *, name):
    names = list(SMALL)
    dm = dmod_all.shape[0] // N_DEV

    def grad_of(nm, red_ref, dmod_ref, ptot_ref):
        if nm == "b_ada":
            acc = None
            for d in range(N_DEV):
                for s in range(nb):
                    blk = dmod_ref[d * dm + 9 * s:d * dm + 9 * s + 9, :]
                    acc = blk if acc is None else acc + blk
            return jnp.concatenate([acc[k:k + 1, :] for k in range(9)], axis=1)
        if nm == "pool_grp":
            return ptot_ref[...]
        row, lo, n = {"norm_ffn1": (0, 0, D), "norm_mix": (1, 0, D), "norm_ffn2": (2, 0, D), "pool_scale": (3, 0, POOL_W),
                      "q_a_norm": (4, 0, QL), "kv_a_norm": (5, 0, KVL), "q_norm_nope": (6, 0, NOPE),
                      "q_norm_rope": (6, NOPE, ROPE), "k_norm_nope": (7, 0, NOPE), "k_norm_rope": (7, KR_LANE, ROPE)}[nm]
        return red_ref[row:row + 1, lo:lo + n]

    def body(*refs):
        red_ref, dmod_ref, ptot_ref = refs[:3]
        ins = refs[3:3 + 3 * len(names)]
        outs = refs[3 + 3 * len(names):]
        outs[4 * len(names)][...] = red_ref[RED_ROWS - 1:RED_ROWS, LOSS_LANE:]
        for i, nm in enumerate(names):
            g = grad_of(nm, red_ref, dmod_ref, ptot_ref)
            d, mn, vn = _adamw_math(ins[3 * i][...], g, ins[3 * i + 1][...], ins[3 * i + 2][...])
            outs[4 * i][...] = g
            outs[4 * i + 1][...] = d
            outs[4 * i + 2][...] = mn
            outs[4 * i + 3][...] = vn

    flat_in = [a for nm in names for a in params[nm]]
    out_shape = [_sds(params[nm][0].shape, F32) for nm in names for _ in range(4)] + [_sds((1, LANE), F32)]
    res = _pcall(body, name=name, in_specs=[VMEM_SPEC] * (3 + len(flat_in)), out_specs=[VMEM_SPEC] * len(out_shape),
                 out_shape=out_shape)(red, dmod_all, pool_total, *flat_in)
    return {nm: tuple(res[4 * i:4 * i + 4]) for i, nm in enumerate(names)}, res[-1]


def _adamw(w, g, m, v, *, name, comm=None):
    r, c = w.shape
    tr = _pick_rows(r, 256)
    tc = c if tr < r else _pick(c, 256)
    spec = pl.BlockSpec((tr, tc), lambda i, j: (i, j))
    bc1 = 1.0 / (1.0 - ADAM_B1 ** ADAM_STEP)
    bc2 = 1.0 / (1.0 - ADAM_B2 ** ADAM_STEP)

    def body(w_ref, g_ref, m_ref, v_ref, d_ref, mo_ref, vo_ref):
        gv = g_ref[...]
        mn = ADAM_B1 * m_ref[...] + (1.0 - ADAM_B1) * gv
        vn = ADAM_B2 * v_ref[...] + (1.0 - ADAM_B2) * (gv * gv)
        mo_ref[...] = mn
        vo_ref[...] = vn
        d_ref[...] = -ADAM_LR * ((mn * bc1) / (jnp.sqrt(vn * bc2) + ADAM_EPS) + ADAM_WD * w_ref[...])

    out = _sds((r, c), F32)
    return _pcall(body, name=name, out_shape=[out, out, out], grid=(r // tr, c // tc), in_specs=[spec] * 4, out_specs=[spec] * 3,
                  comm=comm)(w, g, m, v)


BIG = ("w_ffn1_in", "w_ffn1_out", "w_in", "w_pool_proj", "w_q_up", "w_kv_up", "w_mla_proj", "w_out", "w_ffn2_in", "w_ffn2_out")
ROW_SHARDED = ("w_ffn1_out", "w_out", "w_ffn2_out")
KERNEL_NAME = {"w_ffn1_in": "ffn1_in", "w_ffn1_out": "ffn1_out", "w_in": "w_in", "w_pool_proj": "pool_proj", "w_q_up": "q_up",
               "w_kv_up": "kv_up", "w_mla_proj": "mla_proj", "w_out": "w_out", "w_ffn2_in": "ffn2_in", "w_ffn2_out": "ffn2_out"}
WEIGHTS = ("w_ada", "b_ada", "norm_ffn1", "w_ffn1_in", "w_ffn1_out", "norm_mix", "w_in", "pool_grp", "pool_scale", "w_pool_proj",
           "q_a_norm", "w_q_up", "kv_a_norm", "w_kv_up", "q_norm_nope", "q_norm_rope", "k_norm_nope", "k_norm_rope", "w_mla_proj",
           "w_out", "norm_ffn2", "w_ffn2_in", "w_ffn2_out")
SMALL = ("b_ada", "norm_ffn1", "norm_mix", "pool_grp", "pool_scale", "q_a_norm", "kv_a_norm", "q_norm_nope", "q_norm_rope",
         "k_norm_nope", "k_norm_rope", "norm_ffn2")


def _assemble(name, stacked):
    if name in ROW_SHARDED:
        return stacked.reshape(stacked.shape[0] * stacked.shape[1], stacked.shape[2])
    return jnp.transpose(stacked, (1, 0, 2)).reshape(stacked.shape[1], stacked.shape[0] * stacked.shape[2])


def _split(name, full):
    if name in ROW_SHARDED:
        return full.reshape(N_CHIP, full.shape[0] // N_CHIP, full.shape[1])
    return jnp.transpose(full.reshape(full.shape[0], N_CHIP, full.shape[1] // N_CHIP), (1, 0, 2))


GU = ("w_ffn1_in", "w_ffn2_in")
NARROW = ("w_in", "w_q_up")
TRANSPOSED = ("w_in",)
W_IN_SHARD = (Z_W - LANE + ROPE) // N_CHIP


MIX_SMALL = ("w_out", "w_pool_proj", "w_mla_proj", "w_q_up", "w_kv_up")
RIDES = {
    "ada_prologue": (("gather", ("w_ffn1_in",)),),
    "ffn1_up_a": (("gather", ("w_ffn1_out",)),),
    "ffn1_up_b": (("gather", ("w_in",)),),
    "mix_in": (("gather", MIX_SMALL),),
    "attn_fwd": (("gather", ("w_ffn2_in", "w_ffn2_out")),),
    "d_ffn2_h": (("swap", ("w_ffn2_out", "w_ffn2_in")),),
    "attn_bwd": (("exchange", ("w_ffn2_out", "w_ffn2_in")),),
    "d_mix_in": (("swap", MIX_SMALL),),
    "d_mix_h": (("exchange", MIX_SMALL), ("swap", ("w_in",))),
    "ffn1_dact": (("exchange", ("w_in",)), ("swap", ("w_ffn1_out",))),
    "d_ffn1_in_a": (("exchange", ("w_ffn1_out",)),),
    "d_ffn1_in_b": (("swap", ("w_ffn1_in@0",)),),
    "d_ffn1_h": (("exchange", ("w_ffn1_in@0",)), ("swap", ("w_ffn1_in@1",))),
    "bwd_norm1": (("exchange", ("w_ffn1_in@1",)),),
    "small_allreduce": (("join", tuple(n for n in BIG if n != "w_ffn1_in") + ("w_ffn1_in@0", "w_ffn1_in@1")),),
}


def _kname(n):
    base, _, part = n.partition("@")
    return KERNEL_NAME[base] + ("@" + part if part else "")


def _both(comms):
    if len(comms) == 1:
        return comms[0]
    ins, outs, sems, aliases, spans = [], [], [], {}, []
    for c in comms:
        spans.append((len(ins), len(c.ins), len(outs), len(c.out_shapes), len(sems), len(c.sems)))
        aliases.update({len(ins) + i: len(outs) + o for i, o in c.aliases.items()})
        ins, outs, sems = ins + c.ins, outs + c.out_shapes, sems + c.sems

    def each(which):
        def run(i_, o_, s_):
            for c, (ia, ni, oa, no, sa, ns) in zip(comms, spans):
                getattr(c, which)(i_[ia:ia + ni], o_[oa:oa + no], s_[sa:sa + ns])
        return run

    return _Comm(ins, outs, sems, each("start"), each("finish"), aliases)


class _ExchangePlan:
    def __init__(self, shards, cidx):
        self.shards, self.cidx = shards, cidx
        self.W, self.G, self.parts, self.pre, self.reduced, self.joined = {}, {}, {}, {}, {}, {}

    def grad(self, key, g):
        self.G[key] = g

    def rider(self, name):
        comms = []
        for kind, names in RIDES.get(name, ()):
            if kind == "gather":
                comms.append(_gather_comm([self.shards[n] for n in names]))
            elif kind == "swap":
                for n in names:
                    self.parts[n] = self._stacked(n)
                comms.append(_swap_comm([self.parts[n] for n in names]))
            elif kind == "exchange":
                comms.append(_exchange_comm([self.pre[n] for n in names]))
            else:
                comms.append(_join_comm([self.reduced[n] for n in names]))
        return _both(comms) if comms else None

    def landed(self, name, outs):
        at = 0
        for kind, names in RIDES[name]:
            for n, o in zip(names, outs[at:at + len(names)]):
                if kind == "gather":
                    self.W[KERNEL_NAME[n]] = self._to_kernel(n, o)
                elif kind == "swap":
                    self.pre[n] = _add_half(self.parts[n], o, self.cidx, name="rs_add_" + _kname(n))
                elif kind == "exchange":
                    self.reduced[n] = _sum_chips(o, self.cidx, name="rs_sum_" + _kname(n))
                else:
                    self.joined[n] = o
            at += len(names)

    @staticmethod
    def _to_kernel(n, stacked):
        if n in GU:
            return stacked
        if n in TRANSPOSED:
            return _w_in_stacked_to_kernel(stacked, W_IN_SHARD)
        full = _assemble(n, stacked)
        return {"w_q_up": _q_up_to_kernel, "w_kv_up": _kv_up_to_kernel}.get(n, lambda w: w)(full)

    def _stacked(self, n):
        g = self.G[_kname(n)]
        if n.partition("@")[0] in GU:
            return g
        if n in TRANSPOSED:
            return _w_in_kernel_to_stacked(g, W_IN_SHARD)
        full = {"w_q_up": _q_up_from_kernel, "w_kv_up": _kv_up_from_kernel}.get(n, lambda w: w)(g)
        return _split(n, full)


def kernel(x, c, positions, w_ada, b_ada, norm_ffn1, w_ffn1_in, w_ffn1_out, norm_mix, w_in, pool_grp, pool_scale, w_pool_proj, q_a_norm, w_q_up, kv_a_norm, w_kv_up, q_norm_nope, q_norm_rope, k_norm_nope, k_norm_rope, w_mla_proj, w_out, norm_ffn2, w_ffn2_in, w_ffn2_out, loss_target, m_w_ada, m_b_ada, m_norm_ffn1, m_w_ffn1_in, m_w_ffn1_out, m_norm_mix, m_w_in, m_pool_grp, m_pool_scale, m_w_pool_proj, m_q_a_norm, m_w_q_up, m_kv_a_norm, m_w_kv_up, m_q_norm_nope, m_q_norm_rope, m_k_norm_nope, m_k_norm_rope, m_w_mla_proj, m_w_out, m_norm_ffn2, m_w_ffn2_in, m_w_ffn2_out, v_w_ada, v_b_ada, v_norm_ffn1, v_w_ffn1_in, v_w_ffn1_out, v_norm_mix, v_w_in, v_pool_grp, v_pool_scale, v_w_pool_proj, v_q_a_norm, v_w_q_up, v_kv_a_norm, v_w_kv_up, v_q_norm_nope, v_q_norm_rope, v_k_norm_nope, v_k_norm_rope, v_w_mla_proj, v_w_out, v_norm_ffn2, v_w_ffn2_in, v_w_ffn2_out):
    args = dict(locals())
    wts = {n: args[n][0] for n in WEIGHTS}
    mom = {n: args["m_" + n][0] for n in WEIGHTS}
    var = {n: args["v_" + n][0] for n in WEIGHTS}
    nb, seq, dm = x.shape
    tokens = nb * seq
    xi, yi, ci = _coords()
    chip = 2 * xi + yi

    cidx = ci.astype(jnp.int32).reshape(1)
    late = tuple(n for n in BIG if n != "w_ffn1_in" and n not in TRANSPOSED)
    plan = _ExchangePlan({n: _pad_rows(wts[n].T.astype(MX)) if n in TRANSPOSED else wts[n].astype(MX)
                          for n in BIG if n not in late}, cidx)
    plan.W["pool_grp"] = wts["pool_grp"].astype(MX)

    ncol = w_ada.shape[2]
    b_cols = lax.dynamic_slice_in_dim(wts["b_ada"].reshape(1, -1), chip * ncol, ncol, axis=1)
    c_slots, mod_slots, *rounded = _run(plan, _ada_prologue, c, wts["w_ada"], b_cols, [wts[n] for n in late], name="ada_prologue")
    plan.shards.update(zip(late, rounded))
    c_all = c_slots[:, :nb].reshape(N_DEV * nb, dm)
    mod3 = jnp.transpose(mod_slots[:, :nb], (1, 0, 2)).reshape(nb, 9, dm)
    P = {"norm_ffn1": wts["norm_ffn1"].reshape(1, dm), "norm_mix": wts["norm_mix"].reshape(1, dm),
         "norm_ffn2": wts["norm_ffn2"].reshape(1, dm), "pool_scale": wts["pool_scale"].reshape(1, POOL_W),
         "q_a_norm": wts["q_a_norm"].reshape(1, QL), "kv_a_norm": wts["kv_a_norm"].reshape(1, KVL),
         "q_gain": _gain_slab(wts["q_norm_nope"].reshape(1, NOPE), wts["q_norm_rope"].reshape(1, ROPE)),
         "k_gain": _gain_slab(wts["k_norm_nope"].reshape(1, NOPE), wts["k_norm_rope"].reshape(1, ROPE))}
    cos, sin = _rope_tables(positions.reshape(tokens))

    loss8, grad_x, stats, d_pool_grp = _layer_fwd_bwd(x.reshape(tokens, dm), loss_target.reshape(tokens, dm), mod3, cos, sin, plan, P)

    slab = _pack_stats(*stats, loss8, name="pack_stats")
    red, dmod_rows, pool_total = _run(plan, _small_allreduce, slab, d_pool_grp.reshape(POOL_G * LANE, LANE), name="small_allreduce")
    grads_t = {n: plan.joined[n][:W_IN_SHARD] for n in TRANSPOSED}
    grads = {n: grads_t[n].T if n in TRANSPOSED else plan.joined[n] for n in BIG if n != "w_ffn1_in"}
    grads["w_ffn1_in"] = jnp.concatenate([plan.joined["w_ffn1_in@0"], plan.joined["w_ffn1_in@1"]], axis=0)
    dm_rows = dmod_rows.shape[0] // N_DEV
    dmod_all = dmod_rows.reshape(N_DEV, dm_rows, SLAB_W)[:, :9 * nb].reshape(N_DEV * nb, 9 * dm)
    dmod_cols = lax.dynamic_slice_in_dim(dmod_all, chip * ncol, ncol, axis=1)
    grads["w_ada"] = _ada_bwd(c_all, dmod_cols, name="ada_bwd")

    delta, new_m, new_v = {}, {}, {}
    as2d = lambda a: a.reshape(POOL_G * LANE, LANE) if a.ndim == 4 else a.reshape(1, -1)
    upd, loss_row = _small_update(red, dmod_rows, pool_total, nb,
                                  {n: tuple(as2d(args[p + n]) for p in ("", "m_", "v_")) for n in SMALL}, name="small_update")
    loss = loss_row[0, 0]
    for n in SMALL:
        grads[n], delta[n], new_m[n], new_v[n] = upd[n]
    for n in ("w_ada",) + BIG:
        if n in NARROW:
            res = _adamw(wts[n].T, grads_t[n] if n in TRANSPOSED else grads[n].T, mom[n].T, var[n].T, name="adamw_" + n)
            delta[n], new_m[n], new_v[n] = (r.T for r in res)
        else:
            delta[n], new_m[n], new_v[n] = _adamw(wts[n], grads[n], mom[n], var[n], name="adamw_" + n)

    def lead(a, n):
        return a.reshape((1,) + wts[n].shape)

    return (loss, grad_x.reshape(nb, seq, dm), *[lead(grads[n], n) for n in WEIGHTS], *[lead(delta[n], n) for n in WEIGHTS],
            *[lead(new_m[n], n) for n in WEIGHTS], *[lead(new_v[n], n) for n in WEIGHTS])
```
